```python
import jax, jax.numpy as jnp
from jax import lax
import numpy as np

D_MODEL = 1024
BATCH = 8
SEQ = 4096
DEPTH = 1

HEAD_DIM = 64
DIL_PAIRS = ((128, 1), (512, 4), (2048, 16))
N_DIL_GROUPS = len(DIL_PAIRS)
HEADS_PER_DIL_GROUP = 4
N_DIL_HEADS = N_DIL_GROUPS * HEADS_PER_DIL_GROUP
DIL_WIDTH = N_DIL_HEADS * HEAD_DIM
DIL_OUT_WIDTH = HEADS_PER_DIL_GROUP * HEAD_DIM
N_FOX_HEADS = 8
FOX_WIDTH = N_FOX_HEADS * HEAD_DIM
FOX_BLOCK = 128
N_BRANCHES = 2
D_FF = -(-8 * D_MODEL // (3 * 256)) * 256
RMS_EPS = 1e-6
NEG_INF = -1e30
ATTN_SCALE = HEAD_DIM ** -0.5
IN_COLS = 3 * DIL_WIDTH + 3 * FOX_WIDTH + N_FOX_HEADS + N_BRANCHES * D_MODEL

kernel_name = 'hybrid_dilated_fox_gated_block'


def rms_norm(x, g):
    xf = x.astype(jnp.float32)
    y = xf * lax.rsqrt(jnp.mean(xf * xf, axis=-1, keepdims=True) + RMS_EPS)
    return (y * g.astype(jnp.float32)).astype(x.dtype)


def alibi_slopes(n):
    return 2.0 ** (-8.0 * jnp.arange(1, n + 1, dtype=jnp.float32) / n)


def dilated_window_attention(q, k, v, window, dilation, slopes):
    b, s, h, dh = q.shape
    w = window // dilation
    span = w * dilation
    s_pad = -(-s // span) * span
    nb = s_pad // span
    pad = ((0, 0), (0, s_pad - s), (0, 0), (0, 0))

    def to_blocks(t):
        return jnp.pad(t, pad).reshape(b, nb, w, dilation, h, dh)

    def with_prev(t):
        prev = jnp.pad(t, ((0, 0), (1, 0), (0, 0), (0, 0), (0, 0), (0, 0)))[:, :-1]
        return jnp.concatenate([prev, t], axis=2)

    qb = to_blocks(q)
    kw = with_prev(to_blocks(k))
    vw = with_prev(to_blocks(v))
    scores = jnp.einsum('bnqrhd,bnkrhd->bnrhqk', qb, kw,
                        preferred_element_type=jnp.float32) * ATTN_SCALE
    qi = jnp.arange(w)[:, None]
    ki = jnp.arange(2 * w)[None, :] - w
    rel = qi - ki
    glob = jnp.arange(nb)[:, None] * w + ki
    valid = ((rel >= 0) & (rel <= w))[None] & (glob >= 0)[:, None, :]
    bias = -slopes[:, None, None] * (rel * dilation).astype(jnp.float32)[None]
    logits = jnp.where(valid[None, :, None, None], scores + bias[None, None, None], NEG_INF)
    m = jnp.max(logits, axis=-1, keepdims=True)
    p = jnp.exp(logits - m)
    denom = jnp.sum(p, axis=-1, keepdims=True)
    lse = (m + jnp.log(denom))[..., 0]
    o = jnp.einsum('bnrhqk,bnkrhd->bnqrhd', (p / denom).astype(v.dtype), vw,
                   preferred_element_type=jnp.float32)
    out = o.reshape(b, s_pad, h, dh)[:, :s]
    lse = lse.transpose(0, 1, 4, 2, 3).reshape(b, s_pad, h)[:, :s]
    return out, lse


def forgetting_attention(q, k, v, log_f):
    b, s, h, dh = q.shape
    c = jnp.cumsum(log_f, axis=1)
    s_pad = -(-s // FOX_BLOCK) * FOX_BLOCK
    nblk = s_pad // FOX_BLOCK
    qb = jnp.pad(q, ((0, 0), (0, s_pad - s), (0, 0), (0, 0)))
    qb = qb.reshape(b, nblk, FOX_BLOCK, h, dh).transpose(1, 0, 2, 3, 4)
    cqb = jnp.pad(c, ((0, 0), (0, s_pad - s), (0, 0)))
    cqb = cqb.reshape(b, nblk, FOX_BLOCK, h).transpose(1, 0, 3, 2)
    ck = c.transpose(0, 2, 1)
    starts = jnp.arange(nblk) * FOX_BLOCK
    kpos = jnp.arange(s)

    def block(args):
        q_blk, c_blk, start = args
        sc = jnp.einsum('bqhd,bkhd->bhqk', q_blk, k,
                        preferred_element_type=jnp.float32) * ATTN_SCALE
        decay = c_blk[..., None] - ck[:, :, None, :]
        qpos = start + jnp.arange(FOX_BLOCK)
        mask = kpos[None, :] <= qpos[:, None]
        p = jax.nn.softmax(jnp.where(mask, sc + decay, NEG_INF), axis=-1)
        return jnp.einsum('bhqk,bkhd->bqhd', p.astype(v.dtype), v,
                          preferred_element_type=jnp.float32)

    out = lax.map(block, (qb, cqb, starts))
    return out.transpose(1, 0, 2, 3, 4).reshape(b, s_pad, h, dh)[:, :s]


def _fwd_setup_inputs(seed: int = 0) -> dict:
    key = jax.random.key(seed)
    ks = jax.random.split(key, 13)
    f32 = jnp.float32

    def dense(k, shape, fan_in, gain=1.0):
        return jax.random.normal(k, shape, f32) * (gain * fan_in ** -0.5)

    return {
        'x': jax.random.normal(ks[0], (BATCH, SEQ, D_MODEL), f32),
        'norm_mix_g': 1.0 + 0.01 * jax.random.normal(ks[1], (DEPTH, D_MODEL), f32),
        'w_in': dense(ks[2], (DEPTH, D_MODEL, IN_COLS), D_MODEL),
        'b_fgt': jax.random.uniform(ks[3], (DEPTH, N_FOX_HEADS), f32, 1.0, 4.0),
        'b_gate': 0.01 * jax.random.normal(ks[4], (DEPTH, N_BRANCHES * D_MODEL), f32),
        'w_dil_out': dense(ks[5], (DEPTH, DIL_OUT_WIDTH, D_MODEL), DIL_OUT_WIDTH),
        'w_fox_out': dense(ks[6], (DEPTH, FOX_WIDTH, D_MODEL), FOX_WIDTH),
        'w_out': dense(ks[7], (DEPTH, D_MODEL, D_MODEL), D_MODEL),
        'norm_ffn_g': 1.0 + 0.01 * jax.random.normal(ks[8], (DEPTH, D_MODEL), f32),
        'w_ffn_in': dense(ks[9], (DEPTH, D_MODEL, 2 * D_FF), D_MODEL),
        'w_ffn_down': dense(ks[10], (DEPTH, D_FF, D_MODEL), D_FF),
        'norm_final_g': 1.0 + 0.01 * jax.random.normal(ks[11], (D_MODEL,), f32),
    }


def _fwd_reference(x, norm_mix_g, w_in, b_fgt, b_gate, w_dil_out, w_fox_out, w_out,
              norm_ffn_g, w_ffn_in, w_ffn_down, norm_final_g):
    b, s, _ = x.shape
    slopes = alibi_slopes(N_DIL_HEADS).reshape(N_DIL_GROUPS, HEADS_PER_DIL_GROUP)
    o1 = DIL_WIDTH
    o2 = 3 * DIL_WIDTH
    o3 = o2 + 3 * FOX_WIDTH
    o4 = o3 + N_FOX_HEADS
    for layer in range(DEPTH):
        h = rms_norm(x, norm_mix_g[layer])
        proj = h @ w_in[layer]
        dil = proj[..., :o2].reshape(b, s, 3, N_DIL_GROUPS, HEADS_PER_DIL_GROUP, HEAD_DIM)
        fox = proj[..., o2:o3].reshape(b, s, 3, N_FOX_HEADS, HEAD_DIM)
        f_logit = proj[..., o3:o4].astype(jnp.float32) + b_fgt[layer]
        gates = jax.nn.sigmoid(proj[..., o4:].astype(jnp.float32) + b_gate[layer])
        g_a, g_b = gates[..., :D_MODEL], gates[..., D_MODEL:]

        outs, lses = [], []
        for g, (window, dilation) in enumerate(DIL_PAIRS):
            o_g, lse_g = dilated_window_attention(dil[:, :, 0, g], dil[:, :, 1, g],
                                                  dil[:, :, 2, g], window, dilation, slopes[g])
            outs.append(o_g)
            lses.append(lse_g)
        alpha = jax.nn.softmax(jnp.stack(lses, axis=0), axis=0)
        o_a = jnp.sum(alpha[..., None] * jnp.stack(outs, axis=0), axis=0)
        y_a = o_a.reshape(b, s, DIL_OUT_WIDTH).astype(x.dtype) @ w_dil_out[layer]

        o_b = forgetting_attention(fox[:, :, 0], fox[:, :, 1], fox[:, :, 2],
                                   jax.nn.log_sigmoid(f_logit))
        y_b = o_b.reshape(b, s, FOX_WIDTH).astype(x.dtype) @ w_fox_out[layer]

        merged = (g_a * y_a + g_b * y_b).astype(x.dtype)
        x = x + merged @ w_out[layer]

        h2 = rms_norm(x, norm_ffn_g[layer])
        gu = h2 @ w_ffn_in[layer]
        x = x + (jax.nn.silu(gu[..., :D_FF]) * gu[..., D_FF:]) @ w_ffn_down[layer]
    return rms_norm(x, norm_final_g)


import jax as _jax
import jax.numpy as _jnp

TWIN_FORMAT = 'train_step'
FWD_PARAMS = ['x', 'norm_mix_g', 'w_in', 'b_fgt', 'b_gate', 'w_dil_out', 'w_fox_out', 'w_out', 'norm_ffn_g', 'w_ffn_in', 'w_ffn_down', 'norm_final_g']
TWIN_WEIGHTS = ['norm_mix_g', 'w_in', 'b_fgt', 'b_gate', 'w_dil_out', 'w_fox_out', 'w_out', 'norm_ffn_g', 'w_ffn_in', 'w_ffn_down', 'norm_final_g']
TWIN_DIFF_INPUT = 'x'
TWIN_INPUTS = ['x', 'norm_mix_g', 'w_in', 'b_fgt', 'b_gate', 'w_dil_out', 'w_fox_out', 'w_out', 'norm_ffn_g', 'w_ffn_in', 'w_ffn_down', 'norm_final_g', 'loss_target', 'm_norm_mix_g', 'm_w_in', 'm_b_fgt', 'm_b_gate', 'm_w_dil_out', 'm_w_fox_out', 'm_w_out', 'm_norm_ffn_g', 'm_w_ffn_in', 'm_w_ffn_down', 'm_norm_final_g', 'v_norm_mix_g', 'v_w_in', 'v_b_fgt', 'v_b_gate', 'v_w_dil_out', 'v_w_fox_out', 'v_w_out', 'v_norm_ffn_g', 'v_w_ffn_in', 'v_w_ffn_down', 'v_norm_final_g']
TWIN_OUTPUTS = ['loss', 'grad_x', 'grad_norm_mix_g', 'grad_w_in', 'grad_b_fgt', 'grad_b_gate', 'grad_w_dil_out', 'grad_w_fox_out', 'grad_w_out', 'grad_norm_ffn_g', 'grad_w_ffn_in', 'grad_w_ffn_down', 'grad_norm_final_g', 'delta_norm_mix_g', 'delta_w_in', 'delta_b_fgt', 'delta_b_gate', 'delta_w_dil_out', 'delta_w_fox_out', 'delta_w_out', 'delta_norm_ffn_g', 'delta_w_ffn_in', 'delta_w_ffn_down', 'delta_norm_final_g', 'new_m_norm_mix_g', 'new_m_w_in', 'new_m_b_fgt', 'new_m_b_gate', 'new_m_w_dil_out', 'new_m_w_fox_out', 'new_m_w_out', 'new_m_norm_ffn_g', 'new_m_w_ffn_in', 'new_m_w_ffn_down', 'new_m_norm_final_g', 'new_v_norm_mix_g', 'new_v_w_in', 'new_v_b_fgt', 'new_v_b_gate', 'new_v_w_dil_out', 'new_v_w_fox_out', 'new_v_w_out', 'new_v_norm_ffn_g', 'new_v_w_ffn_in', 'new_v_w_ffn_down', 'new_v_norm_final_g']
TWIN_LEAF_KINDS = {'loss': 'loss', 'grad_x': 'grad_x', 'grad_norm_mix_g': 'grad_w', 'grad_w_in': 'grad_w', 'grad_b_fgt': 'grad_w', 'grad_b_gate': 'grad_w', 'grad_w_dil_out': 'grad_w', 'grad_w_fox_out': 'grad_w', 'grad_w_out': 'grad_w', 'grad_norm_ffn_g': 'grad_w', 'grad_w_ffn_in': 'grad_w', 'grad_w_ffn_down': 'grad_w', 'grad_norm_final_g': 'grad_w', 'delta_norm_mix_g': 'delta_w', 'delta_w_in': 'delta_w', 'delta_b_fgt': 'delta_w', 'delta_b_gate': 'delta_w', 'delta_w_dil_out': 'delta_w', 'delta_w_fox_out': 'delta_w', 'delta_w_out': 'delta_w', 'delta_norm_ffn_g': 'delta_w', 'delta_w_ffn_in': 'delta_w', 'delta_w_ffn_down': 'delta_w', 'delta_norm_final_g': 'delta_w', 'new_m_norm_mix_g': 'new_m', 'new_m_w_in': 'new_m', 'new_m_b_fgt': 'new_m', 'new_m_b_gate': 'new_m', 'new_m_w_dil_out': 'new_m', 'new_m_w_fox_out': 'new_m', 'new_m_w_out': 'new_m', 'new_m_norm_ffn_g': 'new_m', 'new_m_w_ffn_in': 'new_m', 'new_m_w_ffn_down': 'new_m', 'new_m_norm_final_g': 'new_m', 'new_v_norm_mix_g': 'new_v', 'new_v_w_in': 'new_v', 'new_v_b_fgt': 'new_v', 'new_v_b_gate': 'new_v', 'new_v_w_dil_out': 'new_v', 'new_v_w_fox_out': 'new_v', 'new_v_w_out': 'new_v', 'new_v_norm_ffn_g': 'new_v', 'new_v_w_ffn_in': 'new_v', 'new_v_w_ffn_down': 'new_v', 'new_v_norm_final_g': 'new_v'}


def _forward(args):
    return _fwd_reference(*[args[k] for k in FWD_PARAMS])


def _output_shape():
    out = _jax.eval_shape(lambda: _forward(_fwd_setup_inputs(0)))
    return out.shape, out.dtype

N_MICROBATCH = 1
ADAM_LR = 0.001
ADAM_B1 = 0.9
ADAM_B2 = 0.999
ADAM_EPS = 1e-08
ADAM_WD = 0.01
ADAM_STEP = 10
PER_EXAMPLE_BATCH_AXIS = {'x': 0, 'loss_target': 0}
SHARED_INPUTS = []
_WEIGHT_DTYPES = {'norm_mix_g': _jnp.float32, 'w_in': _jnp.float32, 'b_fgt': _jnp.float32, 'b_gate': _jnp.float32, 'w_dil_out': _jnp.float32, 'w_fox_out': _jnp.float32, 'w_out': _jnp.float32, 'norm_ffn_g': _jnp.float32, 'w_ffn_in': _jnp.float32, 'w_ffn_down': _jnp.float32, 'norm_final_g': _jnp.float32}
MOMENT_SCALE = {'norm_mix_g': 8.215071e-02, 'w_in': 3.412398e-02, 'b_fgt': 3.567632e-01, 'b_gate': 1.446566e-02, 'w_dil_out': 3.419830e-02, 'w_fox_out': 4.005305e-02, 'w_out': 5.264977e-02, 'norm_ffn_g': 1.338868e-01, 'w_ffn_in': 5.449523e-02, 'w_ffn_down': 8.896544e-02, 'norm_final_g': 3.202370e+01}


def _to_microbatches(a, axis):
    t = _jnp.moveaxis(a, axis, 0)
    t = t.reshape((N_MICROBATCH, t.shape[0] // N_MICROBATCH) + t.shape[1:])
    return _jnp.moveaxis(t, 1, axis + 1)


def setup_inputs(seed: int = 0) -> dict:
    inp = _fwd_setup_inputs(seed)
    key = _jax.random.fold_in(_jax.random.key(seed), 7919)
    shape, _ = _output_shape()
    out = dict(inp)
    out["loss_target"] = _jax.random.normal(_jax.random.fold_in(key, 0), shape, _jnp.float32)
    for i, name in enumerate(TWIN_WEIGHTS):
        w = inp[name].astype(_jnp.float32)
        if MOMENT_SCALE is None:
            s = _jnp.sqrt(_jnp.mean(_jnp.square(w)) + 1e-30)
        else:
            s = MOMENT_SCALE[name]
        km, kv = _jax.random.split(_jax.random.fold_in(key, i + 1))
        out[name] = w
        out["m_" + name] = s * _jax.random.normal(km, w.shape, _jnp.float32)
        out["v_" + name] = (s * s) * _jax.random.uniform(kv, w.shape, _jnp.float32, 0.5, 1.5)
    if N_MICROBATCH > 1:
        for name, axis in PER_EXAMPLE_BATCH_AXIS.items():
            out[name] = _to_microbatches(out[name], axis)
    return {'x': out['x'], 'norm_mix_g': out['norm_mix_g'], 'w_in': out['w_in'], 'b_fgt': out['b_fgt'], 'b_gate': out['b_gate'], 'w_dil_out': out['w_dil_out'], 'w_fox_out': out['w_fox_out'], 'w_out': out['w_out'], 'norm_ffn_g': out['norm_ffn_g'], 'w_ffn_in': out['w_ffn_in'], 'w_ffn_down': out['w_ffn_down'], 'norm_final_g': out['norm_final_g'], 'loss_target': out['loss_target'], 'm_norm_mix_g': out['m_norm_mix_g'], 'm_w_in': out['m_w_in'], 'm_b_fgt': out['m_b_fgt'], 'm_b_gate': out['m_b_gate'], 'm_w_dil_out': out['m_w_dil_out'], 'm_w_fox_out': out['m_w_fox_out'], 'm_w_out': out['m_w_out'], 'm_norm_ffn_g': out['m_norm_ffn_g'], 'm_w_ffn_in': out['m_w_ffn_in'], 'm_w_ffn_down': out['m_w_ffn_down'], 'm_norm_final_g': out['m_norm_final_g'], 'v_norm_mix_g': out['v_norm_mix_g'], 'v_w_in': out['v_w_in'], 'v_b_fgt': out['v_b_fgt'], 'v_b_gate': out['v_b_gate'], 'v_w_dil_out': out['v_w_dil_out'], 'v_w_fox_out': out['v_w_fox_out'], 'v_w_out': out['v_w_out'], 'v_norm_ffn_g': out['v_norm_ffn_g'], 'v_w_ffn_in': out['v_w_ffn_in'], 'v_w_ffn_down': out['v_w_ffn_down'], 'v_norm_final_g': out['v_norm_final_g']}


def _loss(weights, diff, rest, loss_target):
    with _jax.named_scope("forward"):
        args = {**rest, TWIN_DIFF_INPUT: diff, **{k: w.astype(_WEIGHT_DTYPES[k]) for k, w in weights.items()}}
        y = _forward(args)
    with _jax.named_scope("loss_head"):
        err = _jnp.square(y.astype(_jnp.float32) - loss_target)
        return 0.5 * _jnp.sum(_jnp.mean(err, axis=-1)) if err.ndim else 0.5 * err


def _adamw(w, g, m, v):
    m = ADAM_B1 * m + (1.0 - ADAM_B1) * g
    v = ADAM_B2 * v + (1.0 - ADAM_B2) * _jnp.square(g)
    m_hat = m / (1.0 - ADAM_B1 ** ADAM_STEP)
    v_hat = v / (1.0 - ADAM_B2 ** ADAM_STEP)
    delta = -ADAM_LR * (m_hat / (_jnp.sqrt(v_hat) + ADAM_EPS) + ADAM_WD * w)
    return delta, m, v


def reference(x, norm_mix_g, w_in, b_fgt, b_gate, w_dil_out, w_fox_out, w_out, norm_ffn_g, w_ffn_in, w_ffn_down, norm_final_g, loss_target, m_norm_mix_g, m_w_in, m_b_fgt, m_b_gate, m_w_dil_out, m_w_fox_out, m_w_out, m_norm_ffn_g, m_w_ffn_in, m_w_ffn_down, m_norm_final_g, v_norm_mix_g, v_w_in, v_b_fgt, v_b_gate, v_w_dil_out, v_w_fox_out, v_w_out, v_norm_ffn_g, v_w_ffn_in, v_w_ffn_down, v_norm_final_g):
    given = dict(x=x, norm_mix_g=norm_mix_g, w_in=w_in, b_fgt=b_fgt, b_gate=b_gate, w_dil_out=w_dil_out, w_fox_out=w_fox_out, w_out=w_out, norm_ffn_g=norm_ffn_g, w_ffn_in=w_ffn_in, w_ffn_down=w_ffn_down, norm_final_g=norm_final_g, loss_target=loss_target, m_norm_mix_g=m_norm_mix_g, m_w_in=m_w_in, m_b_fgt=m_b_fgt, m_b_gate=m_b_gate, m_w_dil_out=m_w_dil_out, m_w_fox_out=m_w_fox_out, m_w_out=m_w_out, m_norm_ffn_g=m_norm_ffn_g, m_w_ffn_in=m_w_ffn_in, m_w_ffn_down=m_w_ffn_down, m_norm_final_g=m_norm_final_g, v_norm_mix_g=v_norm_mix_g, v_w_in=v_w_in, v_b_fgt=v_b_fgt, v_b_gate=v_b_gate, v_w_dil_out=v_w_dil_out, v_w_fox_out=v_w_fox_out, v_w_out=v_w_out, v_norm_ffn_g=v_norm_ffn_g, v_w_ffn_in=v_w_ffn_in, v_w_ffn_down=v_w_ffn_down, v_norm_final_g=v_norm_final_g)
    weights = {n: given[n] for n in TWIN_WEIGHTS}
    shared = {n: given[n] for n in SHARED_INPUTS}
    per_example = {n: given[n] for n in ['x']}
    grad_fn = _jax.value_and_grad(_loss, argnums=(0, 1))

    def one_microbatch(ex, loss_target):
        ex = dict(ex)
        diff = ex.pop(TWIN_DIFF_INPUT)
        return grad_fn(weights, diff, {**shared, **ex}, loss_target)

    if N_MICROBATCH == 1:
        loss, (grad_w, grad_x) = one_microbatch(per_example, given["loss_target"])
    else:
        def body(carry, xs):
            loss_sum, grad_sum = carry
            l_k, (gw_k, gx_k) = one_microbatch(xs[0], xs[1])
            with _jax.named_scope("update"):
                return (loss_sum + l_k, _jax.tree.map(_jnp.add, grad_sum, gw_k)), gx_k

        init = (_jnp.zeros((), _jnp.float32), _jax.tree.map(_jnp.zeros_like, weights))
        (loss, grad_w), grad_x = _jax.lax.scan(body, init, (per_example, given["loss_target"]))
    with _jax.named_scope("update"):
        delta_w, new_m, new_v = {}, {}, {}
        for n in TWIN_WEIGHTS:
            delta_w[n], new_m[n], new_v[n] = _adamw(weights[n], grad_w[n], given["m_" + n], given["v_" + n])
    return (loss, grad_x, *[grad_w[n] for n in TWIN_WEIGHTS], *[delta_w[n] for n in TWIN_WEIGHTS],
            *[new_m[n] for n in TWIN_WEIGHTS], *[new_v[n] for n in TWIN_WEIGHTS])
```

```python
import numpy as np
import jax
import jax.numpy as jnp
from jax import lax
from jax.experimental import pallas as pl
from jax.experimental.pallas import tpu as pltpu

F32 = jnp.float32
_CD = jnp.bfloat16

D_MODEL = 1024
HEAD_DIM = 64
DIL_PAIRS = ((128, 1), (512, 4), (2048, 16))
N_DIL_GROUPS = 3
DIL_HEADS = 4
DIL_W = 128
DIL_OUT = DIL_HEADS * HEAD_DIM
DIL_WIDTH = N_DIL_GROUPS * DIL_OUT
N_FOX_HEADS = 8
FOX_WIDTH = N_FOX_HEADS * HEAD_DIM
D_FF = 2816
QKV_COLS = 3 * DIL_WIDTH + 3 * FOX_WIDTH
F_PAD = 128
RMS_EPS = 1e-6
NEG_INF = -1e30
ATTN_SCALE = HEAD_DIM ** -0.5
ADAM_LR, ADAM_B1, ADAM_B2, ADAM_EPS, ADAM_WD, ADAM_STEP = 0.001, 0.9, 0.999, 1e-08, 0.01, 10

VMEM_LIMIT = 48 * 1024 * 1024
MESH = pl.DeviceIdType.MESH
HBM_SPEC = pl.BlockSpec(memory_space=pltpu.HBM)


def _pcall(body, **kw):
    return pl.pallas_call(body, **kw)


def _params(*sem):
    return pltpu.CompilerParams(dimension_semantics=sem, vmem_limit_bytes=VMEM_LIMIT)


def _pick(dim, pref):
    t = (min(pref, dim) // 128) * 128
    while t >= 128:
        if dim % t == 0:
            return t
        t -= 128
    return dim


def _mm(a, b, *, name, ta=False, tb=False, out_dtype=F32, add=None, tm=512, tn=512, tk=2048):
    if ta:
        K, M = a.shape
    else:
        M, K = a.shape
    if tb:
        N, K2 = b.shape
    else:
        K2, N = b.shape
    assert K == K2, (a.shape, b.shape)
    tm, tn, tk = _pick(M, tm), _pick(N, tn), _pick(K, tk)
    nk = K // tk
    dn = (((0 if ta else 1,), (1 if tb else 0,)), ((), ()))
    has_add = add is not None

    def body(*refs):
        a_ref, b_ref = refs[0], refs[1]
        add_ref = refs[2] if has_add else None
        o_ref = refs[3] if has_add else refs[2]
        p = lax.dot_general(a_ref[...].astype(_CD), b_ref[...].astype(_CD), dn, preferred_element_type=F32)

        def finish(r):
            if has_add:
                r = r + add_ref[...]
            o_ref[...] = r.astype(out_dtype)

        if nk == 1:
            finish(p)
        else:
            acc_ref = refs[-1]
            k = pl.program_id(2)

            @pl.when(k == 0)
            def _():
                acc_ref[...] = p

            @pl.when(k > 0)
            def _():
                acc_ref[...] += p

            @pl.when(k == nk - 1)
            def _():
                finish(acc_ref[...])

    a_spec = pl.BlockSpec((tk, tm), lambda i, j, k: (k, i)) if ta else pl.BlockSpec((tm, tk), lambda i, j, k: (i, k))
    b_spec = pl.BlockSpec((tn, tk), lambda i, j, k: (j, k)) if tb else pl.BlockSpec((tk, tn), lambda i, j, k: (k, j))
    o_spec = pl.BlockSpec((tm, tn), lambda i, j, k: (i, j))
    in_specs = [a_spec, b_spec] + ([o_spec] if has_add else [])
    args = (a, b) + ((add,) if has_add else ())
    return _pcall(
        body, name=name, grid=(M // tm, N // tn, nk), in_specs=in_specs, out_specs=o_spec,
        out_shape=jax.ShapeDtypeStruct((M, N), out_dtype),
        scratch_shapes=[pltpu.VMEM((tm, tn), F32)] if nk > 1 else [],
        compiler_params=_params("parallel", "parallel", "arbitrary"),
    )(*args)


def _rms_fwd(x, g, *, name, tm=512):
    S, D = x.shape

    def body(x_ref, g_ref, h_ref):
        xv = x_ref[...]
        r = lax.rsqrt(jnp.mean(xv * xv, axis=-1, keepdims=True) + RMS_EPS)
        h_ref[...] = ((xv * r) * g_ref[...]).astype(h_ref.dtype)

    row = pl.BlockSpec((tm, D), lambda i: (i, 0))
    return _pcall(body, name=name, grid=(S // tm,), in_specs=[row, pl.BlockSpec((1, D), lambda i: (0, 0))],
                  out_specs=row, out_shape=jax.ShapeDtypeStruct((S, D), _CD), compiler_params=_params("parallel"))(x, g)


def _rms_bwd(x, g, dh, dres, *, name, tm=512):
    S, D = x.shape

    def body(x_ref, g_ref, dh_ref, dres_ref, dx_ref, dg_ref):
        xv = x_ref[...]
        r = lax.rsqrt(jnp.mean(xv * xv, axis=-1, keepdims=True) + RMS_EPS)
        xh = xv * r
        dhv = dh_ref[...]
        dxh = dhv * g_ref[...]
        dx_ref[...] = dres_ref[...] + r * (dxh - xh * jnp.mean(dxh * xh, axis=-1, keepdims=True))
        part = jnp.sum(dhv * xh, axis=0, keepdims=True)

        @pl.when(pl.program_id(0) == 0)
        def _():
            dg_ref[...] = part

        @pl.when(pl.program_id(0) > 0)
        def _():
            dg_ref[...] += part

    row = pl.BlockSpec((tm, D), lambda i: (i, 0))
    vec = pl.BlockSpec((1, D), lambda i: (0, 0))
    return _pcall(body, name=name, grid=(S // tm,), in_specs=[row, vec, row, row], out_specs=[row, vec],
                  out_shape=[jax.ShapeDtypeStruct((S, D), F32), jax.ShapeDtypeStruct((1, D), F32)],
                  compiler_params=_params("arbitrary"))(x, g, dh, dres)


def _loss_head(x, g, tgt, *, name, tm=512):
    S, D = x.shape

    def body(x_ref, g_ref, t_ref, loss_ref, dx_ref, dg_ref):
        xv = x_ref[...]
        gv = g_ref[...]
        r = lax.rsqrt(jnp.mean(xv * xv, axis=-1, keepdims=True) + RMS_EPS)
        xh = xv * r
        err = xh * gv - t_ref[...]
        lpart = 0.5 * jnp.sum(jnp.mean(err * err, axis=-1, keepdims=True), axis=0, keepdims=True)
        dy = err * (1.0 / D)
        dxh = dy * gv
        dx_ref[...] = r * (dxh - xh * jnp.mean(dxh * xh, axis=-1, keepdims=True))
        gpart = jnp.sum(dy * xh, axis=0, keepdims=True)

        @pl.when(pl.program_id(0) == 0)
        def _():
            loss_ref[...] = lpart
            dg_ref[...] = gpart

        @pl.when(pl.program_id(0) > 0)
        def _():
            loss_ref[...] += lpart
            dg_ref[...] += gpart

    row = pl.BlockSpec((tm, D), lambda i: (i, 0))
    vec = pl.BlockSpec((1, D), lambda i: (0, 0))
    one = pl.BlockSpec((1, 1), lambda i: (0, 0))
    return _pcall(body, name=name, grid=(S // tm,), in_specs=[row, vec, row], out_specs=[one, row, vec],
                  out_shape=[jax.ShapeDtypeStruct((1, 1), F32), jax.ShapeDtypeStruct((S, D), F32),
                             jax.ShapeDtypeStruct((1, D), F32)],
                  compiler_params=_params("arbitrary"))(x, g, tgt)


def _sigmoid(z):
    return 1.0 / (1.0 + jnp.exp(-z))


def _gate_fwd(gl, bg, ya, yb, *, name, tm=512):
    S, D = ya.shape

    def body(za_ref, zb_ref, ba_ref, bb_ref, ya_ref, yb_ref, o_ref):
        ga = _sigmoid(za_ref[...] + ba_ref[...])
        gb = _sigmoid(zb_ref[...] + bb_ref[...])
        o_ref[...] = (ga * ya_ref[...] + gb * yb_ref[...]).astype(o_ref.dtype)

    lo = pl.BlockSpec((tm, D), lambda i: (i, 0))
    hi = pl.BlockSpec((tm, D), lambda i: (i, 1))
    vlo = pl.BlockSpec((1, D), lambda i: (0, 0))
    vhi = pl.BlockSpec((1, D), lambda i: (0, 1))
    return _pcall(body, name=name, grid=(S // tm,), in_specs=[lo, hi, vlo, vhi, lo, lo], out_specs=lo,
                  out_shape=jax.ShapeDtypeStruct((S, D), _CD), compiler_params=_params("parallel"))(gl, gl, bg, bg, ya, yb)


def _gate_bwd(dm, gl, bg, ya, yb, *, name, tm=256):
    S, D = ya.shape

    def body(dm_ref, za_ref, zb_ref, ba_ref, bb_ref, ya_ref, yb_ref, dya_ref, dyb_ref, dgl_ref, dbg_ref):
        dmv = dm_ref[...]
        ga = _sigmoid(za_ref[...] + ba_ref[...])
        gb = _sigmoid(zb_ref[...] + bb_ref[...])
        dya_ref[...] = (dmv * ga).astype(dya_ref.dtype)
        dyb_ref[...] = (dmv * gb).astype(dyb_ref.dtype)
        dza = dmv * ya_ref[...] * ga * (1.0 - ga)
        dzb = dmv * yb_ref[...] * gb * (1.0 - gb)
        dgl_ref[:, :D] = dza.astype(dgl_ref.dtype)
        dgl_ref[:, D:] = dzb.astype(dgl_ref.dtype)
        pa = jnp.sum(dza, axis=0, keepdims=True)
        pb = jnp.sum(dzb, axis=0, keepdims=True)

        @pl.when(pl.program_id(0) == 0)
        def _():
            dbg_ref[:, :D] = pa
            dbg_ref[:, D:] = pb

        @pl.when(pl.program_id(0) > 0)
        def _():
            dbg_ref[:, :D] += pa
            dbg_ref[:, D:] += pb

    lo = pl.BlockSpec((tm, D), lambda i: (i, 0))
    hi = pl.BlockSpec((tm, D), lambda i: (i, 1))
    vlo = pl.BlockSpec((1, D), lambda i: (0, 0))
    vhi = pl.BlockSpec((1, D), lambda i: (0, 1))
    wide = pl.BlockSpec((tm, 2 * D), lambda i: (i, 0))
    vwide = pl.BlockSpec((1, 2 * D), lambda i: (0, 0))
    return _pcall(body, name=name, grid=(S // tm,), in_specs=[lo, lo, hi, vlo, vhi, lo, lo],
                  out_specs=[lo, lo, wide, vwide],
                  out_shape=[jax.ShapeDtypeStruct((S, D), _CD), jax.ShapeDtypeStruct((S, D), _CD),
                             jax.ShapeDtypeStruct((S, 2 * D), _CD), jax.ShapeDtypeStruct((1, 2 * D), F32)],
                  compiler_params=_params("arbitrary"))(dm, gl, gl, bg, bg, ya, yb)


def _swiglu_fwd(gu, *, name, tm=256):
    S, F2 = gu.shape
    F = F2 // 2

    def body(g_ref, u_ref, o_ref):
        gv = g_ref[...]
        o_ref[...] = (gv * _sigmoid(gv) * u_ref[...]).astype(o_ref.dtype)

    lo = pl.BlockSpec((tm, F), lambda i: (i, 0))
    hi = pl.BlockSpec((tm, F), lambda i: (i, 1))
    return _pcall(body, name=name, grid=(S // tm,), in_specs=[lo, hi], out_specs=lo,
                  out_shape=jax.ShapeDtypeStruct((S, F), _CD), compiler_params=_params("parallel"))(gu, gu)


def _swiglu_bwd(dact, gu, *, name, tm=256):
    S, F2 = gu.shape
    F = F2 // 2

    def body(d_ref, g_ref, u_ref, o_ref):
        dv = d_ref[...]
        gv = g_ref[...]
        sg = _sigmoid(gv)
        o_ref[:, :F] = (dv * u_ref[...] * (sg * (1.0 + gv * (1.0 - sg)))).astype(o_ref.dtype)
        o_ref[:, F:] = (dv * (gv * sg)).astype(o_ref.dtype)

    lo = pl.BlockSpec((tm, F), lambda i: (i, 0))
    hi = pl.BlockSpec((tm, F), lambda i: (i, 1))
    return _pcall(body, name=name, grid=(S // tm,), in_specs=[lo, lo, hi],
                  out_specs=pl.BlockSpec((tm, F2), lambda i: (i, 0)),
                  out_shape=jax.ShapeDtypeStruct((S, F2), _CD), compiler_params=_params("parallel"))(dact, gu, gu)


def _split3(x):
    hi = x.astype(jnp.bfloat16)
    r1 = x - hi.astype(F32)
    mid = r1.astype(jnp.bfloat16)
    lo = (r1 - mid.astype(F32)).astype(jnp.bfloat16)
    return hi, mid, lo


def _ones_dot_left(ones, x):
    return sum(jnp.dot(ones, p, preferred_element_type=F32) for p in _split3(x))


def _ones_dot_right(x, ones):
    return sum(jnp.dot(p, ones, preferred_element_type=F32) for p in _split3(x))


def _head_sum(x):
    n = x.shape[1]
    r = lax.broadcasted_iota(jnp.int32, (n, n), 0) // HEAD_DIM
    c = lax.broadcasted_iota(jnp.int32, (n, n), 1) // HEAD_DIM
    return _ones_dot_right(x, (r == c).astype(jnp.bfloat16))


def _log_sigmoid(z):
    e = jnp.exp(-jnp.abs(z))
    t = 1.0 + e
    log1p_e = jnp.where(t == 1.0, e, jnp.log(t) * (e / jnp.where(t == 1.0, 1.0, t - 1.0)))
    return jnp.minimum(z, 0.0) - log1p_e


def _fox_cumsum(zf, bf, *, name):
    S, W = zf.shape
    nb = S // 128

    def body(z_ref, b_ref, c_ref):
        tri = (lax.broadcasted_iota(jnp.int32, (128, 128), 0) >= lax.broadcasted_iota(jnp.int32, (128, 128), 1))
        tri = tri.astype(jnp.bfloat16)

        def step(i, carry):
            rows = pl.ds(pl.multiple_of(i * 128, 128), 128)
            lf = _log_sigmoid(z_ref[rows, :] + b_ref[...])
            cb = _ones_dot_left(tri, lf) + carry
            c_ref[rows, :] = cb
            return cb[127:128, :]

        lax.fori_loop(0, nb, step, jnp.zeros((1, W), F32))

    return _pcall(body, name=name, out_shape=jax.ShapeDtypeStruct((S, W), F32),
                  compiler_params=pltpu.CompilerParams(vmem_limit_bytes=VMEM_LIMIT))(zf, bf)


def _fox_cumsum_bwd(dc, zf, bf, *, name):
    S, W = zf.shape
    nb = S // 128

    def body(dc_ref, z_ref, b_ref, dz_ref, db_ref):
        tri = (lax.broadcasted_iota(jnp.int32, (128, 128), 0) <= lax.broadcasted_iota(jnp.int32, (128, 128), 1))
        tri = tri.astype(jnp.bfloat16)

        def step(k, carry):
            tail, acc = carry
            i = nb - 1 - k
            rows = pl.ds(pl.multiple_of(i * 128, 128), 128)
            dlf = _ones_dot_left(tri, dc_ref[rows, :]) + tail
            dz = dlf * _sigmoid(-(z_ref[rows, :] + b_ref[...]))
            dz_ref[rows, :] = dz
            return dlf[0:1, :], acc + jnp.sum(dz, axis=0, keepdims=True)

        _, acc = lax.fori_loop(0, nb, step, (jnp.zeros((1, W), F32), jnp.zeros((1, W), F32)))
        db_ref[...] = acc

    return _pcall(body, name=name,
                  out_shape=[jax.ShapeDtypeStruct((S, W), F32), jax.ShapeDtypeStruct((1, W), F32)],
                  compiler_params=pltpu.CompilerParams(vmem_limit_bytes=VMEM_LIMIT))(dc, zf, bf)


def _dil_slopes(group):
    h = np.arange(1, N_DIL_GROUPS * DIL_HEADS + 1, dtype=np.float32)
    s = (np.float32(2.0) ** (np.float32(-8.0) * h / np.float32(N_DIL_GROUPS * DIL_HEADS))).astype(np.float32)
    return [float(v) for v in s.reshape(N_DIL_GROUPS, DIL_HEADS)[group]]


def _dil_tiles(i, n, blocks_per_seq):
    qi = lax.broadcasted_iota(jnp.int32, (DIL_W, DIL_W), 0)
    kj = lax.broadcasted_iota(jnp.int32, (DIL_W, DIL_W), 1)
    first = ((4 * n + i) % blocks_per_seq) == 0
    valid_prev = jnp.logical_and(kj >= qi, jnp.logical_not(first))
    valid_cur = kj <= qi
    rel_prev = (qi - kj + DIL_W).astype(F32)
    rel_cur = (qi - kj).astype(F32)
    return valid_prev, valid_cur, rel_prev, rel_cur


CHUNK = 4 * DIL_W


def _dil_fwd(q, k, v, group, *, name):
    S = q.shape[0]
    dilation = DIL_PAIRS[group][1]
    bps = (S // dilation) // DIL_W
    slopes = _dil_slopes(group)
    nt = (((1,), (1,)), ((), ()))

    def body(q_ref, k_ref, v_ref, kp_ref, vp_ref, o_ref, l_ref):
        n = pl.program_id(0)
        for i in range(4):
            valid_prev, valid_cur, rel_prev, rel_cur = _dil_tiles(i, n, bps)
            rows = slice(i * DIL_W, (i + 1) * DIL_W)
            prow = slice((i - 1) * DIL_W, i * DIL_W)
            for h in range(DIL_HEADS):
                cols = slice(h * HEAD_DIM, (h + 1) * HEAD_DIM)
                qh = q_ref[rows, cols]
                kc, vc = k_ref[rows, cols], v_ref[rows, cols]
                kp = kp_ref[:, cols] if i == 0 else k_ref[prow, cols]
                vp = vp_ref[:, cols] if i == 0 else v_ref[prow, cols]
                sl = slopes[h] * dilation
                sp = lax.dot_general(qh, kp, nt, preferred_element_type=F32) * ATTN_SCALE - sl * rel_prev
                sc = lax.dot_general(qh, kc, nt, preferred_element_type=F32) * ATTN_SCALE - sl * rel_cur
                sp = jnp.where(valid_prev, sp, NEG_INF)
                sc = jnp.where(valid_cur, sc, NEG_INF)
                m = jnp.maximum(jnp.max(sp, axis=-1, keepdims=True), jnp.max(sc, axis=-1, keepdims=True))
                pp, pc = jnp.exp(sp - m), jnp.exp(sc - m)
                den = jnp.sum(pp, axis=-1, keepdims=True) + jnp.sum(pc, axis=-1, keepdims=True)
                acc = (jnp.dot(pp.astype(_CD), vp, preferred_element_type=F32)
                       + jnp.dot(pc.astype(_CD), vc, preferred_element_type=F32))
                o_ref[rows, cols] = acc / den
                l_ref[rows, cols] = jnp.broadcast_to(m + jnp.log(den), (DIL_W, HEAD_DIM))

    cur = pl.BlockSpec((CHUNK, DIL_OUT), lambda n: (n, 0))
    prev = pl.BlockSpec((DIL_W, DIL_OUT), lambda n: (jnp.maximum(4 * n - 1, 0), 0))
    return _pcall(body, name=name, grid=(S // CHUNK,), in_specs=[cur, cur, cur, prev, prev], out_specs=[cur, cur],
                  out_shape=[jax.ShapeDtypeStruct((S, DIL_OUT), F32), jax.ShapeDtypeStruct((S, DIL_OUT), F32)],
                  compiler_params=_params("parallel"))(q, k, v, k, v)


def _dil_bwd(q, k, v, o, lse, do, dlse, group, *, name):
    S = q.shape[0]
    dilation = DIL_PAIRS[group][1]
    bps = (S // dilation) // DIL_W
    slopes = _dil_slopes(group)
    nchunk = S // CHUNK
    nt = (((1,), (1,)), ((), ()))
    tn = (((0,), (0,)), ((), ()))

    def body(q_ref, k_ref, v_ref, kp_ref, vp_ref, o_ref, l_ref, do_ref, dl_ref, dq_ref, dk_ref, dv_ref,
             dk_s, dv_s):
        step = pl.program_id(0)
        n = nchunk - 1 - step

        @pl.when(step == 0)
        def _():
            dk_s[CHUNK:, :] = jnp.zeros((DIL_W, DIL_OUT), F32)
            dv_s[CHUNK:, :] = jnp.zeros((DIL_W, DIL_OUT), F32)

        dk_s[:CHUNK, :] = jnp.zeros((CHUNK, DIL_OUT), F32)
        dv_s[:CHUNK, :] = jnp.zeros((CHUNK, DIL_OUT), F32)
        for i in range(4):
            valid_prev, valid_cur, rel_prev, rel_cur = _dil_tiles(i, n, bps)
            rows = slice(i * DIL_W, (i + 1) * DIL_W)
            prow = slice((i - 1) * DIL_W, i * DIL_W)
            s_prev = slice(i * DIL_W, (i + 1) * DIL_W)
            s_cur = slice((i + 1) * DIL_W, (i + 2) * DIL_W)
            for h in range(DIL_HEADS):
                cols = slice(h * HEAD_DIM, (h + 1) * HEAD_DIM)
                qh = q_ref[rows, cols]
                kc, vc = k_ref[rows, cols], v_ref[rows, cols]
                kp = kp_ref[:, cols] if i == 0 else k_ref[prow, cols]
                vp = vp_ref[:, cols] if i == 0 else v_ref[prow, cols]
                sl = slopes[h] * dilation
                lh = l_ref[rows, h * HEAD_DIM:h * HEAD_DIM + 1]
                sp = lax.dot_general(qh, kp, nt, preferred_element_type=F32) * ATTN_SCALE - sl * rel_prev
                sc = lax.dot_general(qh, kc, nt, preferred_element_type=F32) * ATTN_SCALE - sl * rel_cur
                pp = jnp.exp(jnp.where(valid_prev, sp, NEG_INF) - lh)
                pc = jnp.exp(jnp.where(valid_cur, sc, NEG_INF) - lh)
                doh = do_ref[rows, cols]
                dsum = jnp.sum(doh * o_ref[rows, cols], axis=-1, keepdims=True)
                shift = dl_ref[rows, h * HEAD_DIM:h * HEAD_DIM + 1] - dsum
                dob = doh.astype(_CD)
                dsp = pp * (lax.dot_general(dob, vp, nt, preferred_element_type=F32) + shift)
                dsc = pc * (lax.dot_general(dob, vc, nt, preferred_element_type=F32) + shift)
                dspb = (dsp * ATTN_SCALE).astype(_CD)
                dscb = (dsc * ATTN_SCALE).astype(_CD)
                dq_ref[rows, cols] = (jnp.dot(dspb, kp, preferred_element_type=F32)
                                      + jnp.dot(dscb, kc, preferred_element_type=F32)).astype(dq_ref.dtype)
                dk_s[s_prev, cols] += lax.dot_general(dspb, qh, tn, preferred_element_type=F32)
                dk_s[s_cur, cols] += lax.dot_general(dscb, qh, tn, preferred_element_type=F32)
                dv_s[s_prev, cols] += lax.dot_general(pp.astype(_CD), dob, tn, preferred_element_type=F32)
                dv_s[s_cur, cols] += lax.dot_general(pc.astype(_CD), dob, tn, preferred_element_type=F32)
        dk_ref[...] = dk_s[DIL_W:, :].astype(dk_ref.dtype)
        dv_ref[...] = dv_s[DIL_W:, :].astype(dv_ref.dtype)
        dk_s[CHUNK:, :] = dk_s[:DIL_W, :]
        dv_s[CHUNK:, :] = dv_s[:DIL_W, :]

    cur = pl.BlockSpec((CHUNK, DIL_OUT), lambda s: (nchunk - 1 - s, 0))
    prev = pl.BlockSpec((DIL_W, DIL_OUT), lambda s: (jnp.maximum(4 * (nchunk - 1 - s) - 1, 0), 0))
    shp = jax.ShapeDtypeStruct((S, DIL_OUT), _CD)
    return _pcall(body, name=name, grid=(nchunk,), in_specs=[cur, cur, cur, prev, prev, cur, cur, cur, cur],
                  out_specs=[cur, cur, cur], out_shape=[shp, shp, shp],
                  scratch_shapes=[pltpu.VMEM((CHUNK + DIL_W, DIL_OUT), F32), pltpu.VMEM((CHUNK + DIL_W, DIL_OUT), F32)],
                  compiler_params=_params("arbitrary"))(q, k, v, k, v, o, lse, do, dlse)


def _dil_mix_fwd(os_, ls_, *, name, tm=512):
    S, W = os_[0].shape

    def body(o0, o1, o2, l0, l1, l2, out_ref):
        ls = [l0[...], l1[...], l2[...]]
        m = jnp.maximum(jnp.maximum(ls[0], ls[1]), ls[2])
        es = [jnp.exp(l - m) for l in ls]
        den = es[0] + es[1] + es[2]
        out_ref[...] = ((es[0] * o0[...] + es[1] * o1[...] + es[2] * o2[...]) / den).astype(out_ref.dtype)

    row = pl.BlockSpec((tm, W), lambda i: (i, 0))
    return _pcall(body, name=name, grid=(S // tm,), in_specs=[row] * 6, out_specs=row,
                  out_shape=jax.ShapeDtypeStruct((S, W), _CD), compiler_params=_params("parallel"))(*os_, *ls_)


def _dil_mix_bwd(doa, os_, ls_, *, name, tm=512):
    S, W = doa.shape

    def body(d_ref, o0, o1, o2, l0, l1, l2, do0, do1, do2, dl0, dl1, dl2):
        dv = d_ref[...]
        ls = [l0[...], l1[...], l2[...]]
        m = jnp.maximum(jnp.maximum(ls[0], ls[1]), ls[2])
        es = [jnp.exp(l - m) for l in ls]
        den = es[0] + es[1] + es[2]
        al = [e / den for e in es]
        da = [_head_sum(dv * o[...]) for o in (o0, o1, o2)]
        mean = al[0] * da[0] + al[1] * da[1] + al[2] * da[2]
        for a, d_, do_ref, dl_ref in zip(al, da, (do0, do1, do2), (dl0, dl1, dl2)):
            do_ref[...] = a * dv
            dl_ref[...] = a * (d_ - mean)

    row = pl.BlockSpec((tm, W), lambda i: (i, 0))
    shp = jax.ShapeDtypeStruct((S, W), F32)
    return _pcall(body, name=name, grid=(S // tm,), in_specs=[row] * 7, out_specs=[row] * 6, out_shape=[shp] * 6,
                  compiler_params=_params("parallel"))(doa, *os_, *ls_)


FOX_T = 512


def _fox_scores(q, k, cq, ck, i, j):
    s = lax.dot_general(q, k, (((1,), (1,)), ((), ())), preferred_element_type=F32) * ATTN_SCALE + (cq - ck)
    qpos = i * FOX_T + lax.broadcasted_iota(jnp.int32, (FOX_T, FOX_T), 0)
    kpos = j * FOX_T + lax.broadcasted_iota(jnp.int32, (FOX_T, FOX_T), 1)
    return jnp.where(kpos <= qpos, s, NEG_INF)


def _fox_fwd(q, k, v, cq, ck, *, name):
    H, S, dh = q.shape
    nt = S // FOX_T

    def body(q_ref, k_ref, v_ref, cq_ref, ck_ref, o_ref, l_ref, m_s, l_s, acc_s):
        i, j = pl.program_id(1), pl.program_id(2)

        @pl.when(j == 0)
        def _():
            m_s[...] = jnp.full((FOX_T, 1), NEG_INF, F32)
            l_s[...] = jnp.zeros((FOX_T, 1), F32)
            acc_s[...] = jnp.zeros((FOX_T, dh), F32)

        @pl.when(j <= i)
        def _():
            s = _fox_scores(q_ref[0], k_ref[0], cq_ref[0], ck_ref[0], i, j)
            m_new = jnp.maximum(m_s[...], jnp.max(s, axis=-1, keepdims=True))
            alpha = jnp.exp(m_s[...] - m_new)
            p = jnp.exp(s - m_new)
            l_s[...] = alpha * l_s[...] + jnp.sum(p, axis=-1, keepdims=True)
            acc_s[...] = alpha * acc_s[...] + jnp.dot(p.astype(_CD), v_ref[0], preferred_element_type=F32)
            m_s[...] = m_new

        @pl.when(j == nt - 1)
        def _():
            o_ref[0] = acc_s[...] / l_s[...]
            l_ref[0] = m_s[...] + jnp.log(l_s[...])

    qs = pl.BlockSpec((1, FOX_T, dh), lambda h, i, j: (h, i, 0))
    ks = pl.BlockSpec((1, FOX_T, dh), lambda h, i, j: (h, jnp.minimum(i, j), 0))
    cqs = pl.BlockSpec((1, FOX_T, 1), lambda h, i, j: (h, i, 0))
    cks = pl.BlockSpec((1, 1, FOX_T), lambda h, i, j: (h, 0, jnp.minimum(i, j)))
    return _pcall(body, name=name, grid=(H, nt, nt), in_specs=[qs, ks, ks, cqs, cks], out_specs=[qs, cqs],
                  out_shape=[jax.ShapeDtypeStruct((H, S, dh), F32), jax.ShapeDtypeStruct((H, S, 1), F32)],
                  scratch_shapes=[pltpu.VMEM((FOX_T, 1), F32), pltpu.VMEM((FOX_T, 1), F32), pltpu.VMEM((FOX_T, dh), F32)],
                  compiler_params=_params("parallel", "parallel", "arbitrary"))(q, k, v, cq, ck)


def _fox_bwd_q(q, k, v, cq, ck, o, lse, do, *, name):
    H, S, dh = q.shape
    nt = S // FOX_T

    def body(q_ref, k_ref, v_ref, cq_ref, ck_ref, o_ref, l_ref, do_ref, dq_ref, dcq_ref, ds_ref, dq_s, dc_s):
        i, j = pl.program_id(1), pl.program_id(2)

        @pl.when(j == 0)
        def _():
            dq_s[...] = jnp.zeros((FOX_T, dh), F32)
            dc_s[...] = jnp.zeros((FOX_T, 1), F32)
            ds_ref[0] = jnp.sum(do_ref[0] * o_ref[0], axis=-1, keepdims=True)

        @pl.when(j <= i)
        def _():
            s = _fox_scores(q_ref[0], k_ref[0], cq_ref[0], ck_ref[0], i, j)
            p = jnp.exp(s - l_ref[0])
            dp = lax.dot_general(do_ref[0].astype(_CD), v_ref[0], (((1,), (1,)), ((), ())), preferred_element_type=F32)
            ds = p * (dp - ds_ref[0])
            dc_s[...] += jnp.sum(ds, axis=-1, keepdims=True)
            dq_s[...] += jnp.dot((ds * ATTN_SCALE).astype(_CD), k_ref[0], preferred_element_type=F32)

        @pl.when(j == nt - 1)
        def _():
            dq_ref[0] = dq_s[...].astype(dq_ref.dtype)
            dcq_ref[0] = dc_s[...]

    qs = pl.BlockSpec((1, FOX_T, dh), lambda h, i, j: (h, i, 0))
    ks = pl.BlockSpec((1, FOX_T, dh), lambda h, i, j: (h, jnp.minimum(i, j), 0))
    cqs = pl.BlockSpec((1, FOX_T, 1), lambda h, i, j: (h, i, 0))
    cks = pl.BlockSpec((1, 1, FOX_T), lambda h, i, j: (h, 0, jnp.minimum(i, j)))
    col = jax.ShapeDtypeStruct((H, S, 1), F32)
    return _pcall(body, name=name, grid=(H, nt, nt), in_specs=[qs, ks, ks, cqs, cks, qs, cqs, qs],
                  out_specs=[qs, cqs, cqs], out_shape=[jax.ShapeDtypeStruct((H, S, dh), _CD), col, col],
                  scratch_shapes=[pltpu.VMEM((FOX_T, dh), F32), pltpu.VMEM((FOX_T, 1), F32)],
                  compiler_params=_params("parallel", "parallel", "arbitrary"))(q, k, v, cq, ck, o, lse, do)


def _fox_bwd_kv(q, k, v, cq, ck, lse, do, dsum, *, name):
    H, S, dh = q.shape
    nt = S // FOX_T
    tn = (((0,), (0,)), ((), ()))

    def body(q_ref, k_ref, v_ref, cq_ref, ck_ref, l_ref, do_ref, ds_ref, dk_ref, dv_ref, dck_ref, dk_s, dv_s, dc_s):
        j, i = pl.program_id(1), pl.program_id(2)

        @pl.when(i == 0)
        def _():
            dk_s[...] = jnp.zeros((FOX_T, dh), F32)
            dv_s[...] = jnp.zeros((FOX_T, dh), F32)
            dc_s[...] = jnp.zeros((1, FOX_T), F32)

        @pl.when(i >= j)
        def _():
            s = _fox_scores(q_ref[0], k_ref[0], cq_ref[0], ck_ref[0], i, j)
            p = jnp.exp(s - l_ref[0])
            dob = do_ref[0].astype(_CD)
            dp = lax.dot_general(dob, v_ref[0], (((1,), (1,)), ((), ())), preferred_element_type=F32)
            ds = p * (dp - ds_ref[0])
            dc_s[...] += jnp.sum(ds, axis=0, keepdims=True)
            dk_s[...] += lax.dot_general((ds * ATTN_SCALE).astype(_CD), q_ref[0], tn, preferred_element_type=F32)
            dv_s[...] += lax.dot_general(p.astype(_CD), dob, tn, preferred_element_type=F32)

        @pl.when(i == nt - 1)
        def _():
            dk_ref[0] = dk_s[...].astype(dk_ref.dtype)
            dv_ref[0] = dv_s[...].astype(dv_ref.dtype)
            dck_ref[0] = dc_s[...]

    ks = pl.BlockSpec((1, FOX_T, dh), lambda h, j, i: (h, j, 0))
    qs = pl.BlockSpec((1, FOX_T, dh), lambda h, j, i: (h, jnp.maximum(i, j), 0))
    cqs = pl.BlockSpec((1, FOX_T, 1), lambda h, j, i: (h, jnp.maximum(i, j), 0))
    cks = pl.BlockSpec((1, 1, FOX_T), lambda h, j, i: (h, 0, j))
    shp = jax.ShapeDtypeStruct((H, S, dh), _CD)
    return _pcall(body, name=name, grid=(H, nt, nt), in_specs=[qs, ks, ks, cqs, cks, cqs, qs, cqs],
                  out_specs=[ks, ks, cks], out_shape=[shp, shp, jax.ShapeDtypeStruct((H, 1, S), F32)],
                  scratch_shapes=[pltpu.VMEM((FOX_T, dh), F32), pltpu.VMEM((FOX_T, dh), F32), pltpu.VMEM((1, FOX_T), F32)],
                  compiler_params=_params("parallel", "parallel", "arbitrary"))(q, k, v, cq, ck, lse, do, dsum)


def _dedilate(t, d):
    if d == 1:
        return t
    S, C = t.shape
    return t.reshape(S // d, d, C).transpose(1, 0, 2).reshape(S, C)


def _redilate(t, d):
    if d == 1:
        return t
    S, C = t.shape
    return t.reshape(d, S // d, C).transpose(1, 0, 2).reshape(S, C)


def _to_heads(t):
    S, W = t.shape
    return t.reshape(S, W // HEAD_DIM, HEAD_DIM).transpose(1, 0, 2)


def _from_heads(t):
    H, S, dh = t.shape
    return t.transpose(1, 0, 2).reshape(S, H * dh)


def _layer_step(x, tgt, w, p):
    S = x.shape[0]
    h = _rms_fwd(x, p["norm_mix_g"], name="rms_mix")
    qkv = _mm(h, w["qkv"], name="proj_qkv", out_dtype=_CD, tn=768)
    zf = _mm(h, w["f"], name="proj_f")
    gl = _mm(h, w["g"], name="proj_gate")

    dil_q, dil_k, dil_v = [], [], []
    dil_o, dil_l = [], []
    for g, (_, d) in enumerate(DIL_PAIRS):
        qg = _dedilate(qkv[:, g * DIL_OUT:(g + 1) * DIL_OUT], d)
        kg = _dedilate(qkv[:, DIL_WIDTH + g * DIL_OUT:DIL_WIDTH + (g + 1) * DIL_OUT], d)
        vg = _dedilate(qkv[:, 2 * DIL_WIDTH + g * DIL_OUT:2 * DIL_WIDTH + (g + 1) * DIL_OUT], d)
        og, lg = _dil_fwd(qg, kg, vg, g, name=f"dil_fwd{g}")
        dil_q.append(qg), dil_k.append(kg), dil_v.append(vg)
        dil_o.append(_redilate(og, d)), dil_l.append(_redilate(lg, d))
    o_a = _dil_mix_fwd(dil_o, dil_l, name="dil_mix")

    base = 3 * DIL_WIDTH
    fq = _to_heads(qkv[:, base:base + FOX_WIDTH])
    fk = _to_heads(qkv[:, base + FOX_WIDTH:base + 2 * FOX_WIDTH])
    fv = _to_heads(qkv[:, base + 2 * FOX_WIDTH:base + 3 * FOX_WIDTH])
    c = _fox_cumsum(zf, p["b_fgt"], name="fox_cumsum")
    ct = c[:, :N_FOX_HEADS].T
    cq, ck = ct[:, :, None], ct[:, None, :]
    fo, flse = _fox_fwd(fq, fk, fv, cq, ck, name="fox_fwd")
    o_b = _from_heads(fo).astype(_CD)

    y_a = _mm(o_a, w["dil_out"], name="y_a")
    y_b = _mm(o_b, w["fox_out"], name="y_b")
    merged = _gate_fwd(gl, p["b_gate"], y_a, y_b, name="gate_fwd")
    x1 = _mm(merged, w["out"], name="mix_out", add=x)

    h2 = _rms_fwd(x1, p["norm_ffn_g"], name="rms_ffn")
    gu = _mm(h2, w["ffn_in"], name="ffn_in")
    act = _swiglu_fwd(gu, name="swiglu")
    x2 = _mm(act, w["ffn_down"], name="ffn_down", add=x1)

    loss, dx2, dg_final = _loss_head(x2, p["norm_final_g"], tgt, name="loss_head")

    dact = _mm(dx2, w["ffn_down"], name="d_act", tb=True, tn=1408)
    gw_ffn_down = _mm(act, dx2, name="gw_ffn_down", ta=True, out_dtype=_CD, tm=1408, tk=1024)
    dgu = _swiglu_bwd(dact, gu, name="swiglu_bwd")
    dh2 = _mm(dgu, w["ffn_in"], name="d_h2", tb=True, tk=1408)
    gw_ffn_in = _mm(h2, dgu, name="gw_ffn_in", ta=True, out_dtype=_CD, tk=1024)
    dx1, dg_ffn = _rms_bwd(x1, p["norm_ffn_g"], dh2, dx2, name="rms_ffn_bwd")

    dmerged = _mm(dx1, w["out"], name="d_merged", tb=True)
    gw_out = _mm(merged, dx1, name="gw_out", ta=True, out_dtype=_CD, tk=1024)
    dy_a, dy_b, dgl, db_gate = _gate_bwd(dmerged, gl, p["b_gate"], y_a, y_b, name="gate_bwd")
    do_a = _mm(dy_a, w["dil_out"], name="d_o_a", tb=True)
    gw_dil_out = _mm(o_a, dy_a, name="gw_dil_out", ta=True, out_dtype=_CD, tk=1024)
    do_b = _mm(dy_b, w["fox_out"], name="d_o_b", tb=True)
    gw_fox_out = _mm(o_b, dy_b, name="gw_fox_out", ta=True, out_dtype=_CD, tk=1024)

    fdo = _to_heads(do_b)
    fdq, dcq, dsum = _fox_bwd_q(fq, fk, fv, cq, ck, fo, flse, fdo, name="fox_bwd_q")
    fdk, fdv, dck = _fox_bwd_kv(fq, fk, fv, cq, ck, flse, fdo, dsum, name="fox_bwd_kv")
    dc = (dcq[:, :, 0] - dck[:, 0, :]).T
    dc = jnp.pad(dc, ((0, 0), (0, F_PAD - N_FOX_HEADS)))
    dzf, db_fgt = _fox_cumsum_bwd(dc, zf, p["b_fgt"], name="fox_cumsum_bwd")

    douts = _dil_mix_bwd(do_a, dil_o, dil_l, name="dil_mix_bwd")
    dqs, dks, dvs = [], [], []
    for g, (_, d) in enumerate(DIL_PAIRS):
        dq, dk, dv = _dil_bwd(dil_q[g], dil_k[g], dil_v[g], _dedilate(dil_o[g], d), _dedilate(dil_l[g], d),
                              _dedilate(douts[g], d), _dedilate(douts[3 + g], d), g, name=f"dil_bwd{g}")
        dqs.append(_redilate(dq, d)), dks.append(_redilate(dk, d)), dvs.append(_redilate(dv, d))
    dqkv = jnp.concatenate(dqs + dks + dvs + [_from_heads(fdq), _from_heads(fdk), _from_heads(fdv)], axis=1)

    dh = _mm(dqkv, w["qkv"], name="d_h_qkv", tb=True, tk=1920)
    dh = _mm(dgl, w["g"], name="d_h_gate", tb=True, add=dh)
    dh = _mm(dzf, w["f"], name="d_h_f", tb=True, add=dh)
    gw_qkv = _mm(h, dqkv, name="gw_qkv", ta=True, out_dtype=_CD, tn=768, tk=1024)
    gw_g = _mm(h, dgl, name="gw_gate", ta=True, out_dtype=_CD, tk=1024)
    gw_f = _mm(h, dzf, name="gw_f", ta=True, out_dtype=_CD, tk=1024)
    dx, dg_mix = _rms_bwd(x, p["norm_mix_g"], dh, dx1, name="rms_mix_bwd")

    gw = dict(qkv=gw_qkv, f=gw_f, g=gw_g, dil_out=gw_dil_out, fox_out=gw_fox_out, out=gw_out, ffn_in=gw_ffn_in,
              ffn_down=gw_ffn_down)
    small = dict(norm_mix_g=dg_mix, b_fgt=db_fgt, b_gate=db_gate, norm_ffn_g=dg_ffn, norm_final_g=dg_final)
    return loss, dx, gw, small


def _position():
    return lax.axis_index("x"), lax.axis_index("y"), lax.axis_index("c")


def _other_chips(x, y):
    return [(1 - x, y), (x, 1 - y), (1 - x, 1 - y)]


def _gather_weights(shards):
    n = len(shards)

    def body(*refs):
        ins, outs = refs[:n], refs[n:2 * n]
        send_sems, recv_sems, local_sems = refs[2 * n:]
        x, y, c = _position()
        mine = 2 * x + y
        chips = _other_chips(x, y)
        started, local = [], []
        for w in range(n):
            half = ins[w].shape[0] // 2
            own = pltpu.make_async_copy(ins[w], outs[w].at[mine], local_sems.at[w])
            own.start()
            local.append(own)
            rows = pl.ds(c * half, half)
            for r, (cx, cy) in enumerate(chips):
                cp = pltpu.make_async_remote_copy(
                    src_ref=ins[w].at[rows, :], dst_ref=outs[w].at[mine, rows, :], send_sem=send_sems.at[w, r],
                    recv_sem=recv_sems.at[w, r], device_id=(cx, cy, c), device_id_type=MESH)
                cp.start()
                started.append(cp)

        def landed(w, r, rows, peer):
            cx, cy = chips[r % 3]
            blk = outs[w].at[2 * cx + cy, rows, :]
            return pltpu.make_async_remote_copy(src_ref=blk, dst_ref=blk, send_sem=send_sems.at[w, r],
                                                recv_sem=recv_sems.at[w, r], device_id=peer, device_id_type=MESH)

        for w in range(n):
            half = ins[w].shape[0] // 2
            rows = pl.ds(c * half, half)
            for r, (cx, cy) in enumerate(chips):
                landed(w, r, rows, (cx, cy, c)).wait_recv()
                fwd = landed(w, 3 + r, rows, (x, y, 1 - c))
                fwd.start()
                started.append(fwd)
        for w in range(n):
            half = ins[w].shape[0] // 2
            rows = pl.ds((1 - c) * half, half)
            for r in range(3):
                landed(w, 3 + r, rows, (x, y, 1 - c)).wait_recv()
        for cp in started:
            cp.wait_send()
        for cp in local:
            cp.wait()

    return _pcall(
        body, name="gather_weights", in_specs=[HBM_SPEC] * n, out_specs=[HBM_SPEC] * n,
        out_shape=[jax.ShapeDtypeStruct((4,) + s.shape, s.dtype) for s in shards],
        scratch_shapes=[pltpu.SemaphoreType.DMA((n, 6)), pltpu.SemaphoreType.DMA((n, 6)), pltpu.SemaphoreType.DMA((n,))],
    )(*shards)


def _swap_halves(grads):
    n = len(grads)

    def body(*refs):
        ins, outs = refs[:n], refs[n:2 * n]
        send_sems, recv_sems = refs[2 * n:]
        x, y, c = _position()
        copies = []
        for w in range(n):
            half = ins[w].shape[1] // 2
            cp = pltpu.make_async_remote_copy(
                src_ref=ins[w].at[:, pl.ds((1 - c) * half, half), :], dst_ref=outs[w], send_sem=send_sems.at[w],
                recv_sem=recv_sems.at[w], device_id=(x, y, 1 - c), device_id_type=MESH)
            cp.start()
            copies.append(cp)
        for cp in copies:
            cp.wait()

    return _pcall(
        body, name="swap_halves", in_specs=[HBM_SPEC] * n, out_specs=[HBM_SPEC] * n,
        out_shape=[jax.ShapeDtypeStruct((4, g.shape[1] // 2, g.shape[2]), g.dtype) for g in grads],
        scratch_shapes=[pltpu.SemaphoreType.DMA((n,)), pltpu.SemaphoreType.DMA((n,))],
    )(*grads)


def _scatter_to_owners(parts):
    n = len(parts)

    def body(*refs):
        ins, outs = refs[:n], refs[n:2 * n]
        send_sems, recv_sems = refs[2 * n:]
        x, y, c = _position()
        copies = []
        for w in range(n):
            for r, (cx, cy) in enumerate(_other_chips(x, y)):
                cp = pltpu.make_async_remote_copy(
                    src_ref=ins[w].at[2 * cx + cy], dst_ref=outs[w].at[r], send_sem=send_sems.at[w, r],
                    recv_sem=recv_sems.at[w, r], device_id=(cx, cy, c), device_id_type=MESH)
                cp.start()
                copies.append(cp)
        for cp in copies:
            cp.wait()

    return _pcall(
        body, name="scatter_to_owners", in_specs=[HBM_SPEC] * n, out_specs=[HBM_SPEC] * n,
        out_shape=[jax.ShapeDtypeStruct((3,) + p.shape[1:], p.dtype) for p in parts],
        scratch_shapes=[pltpu.SemaphoreType.DMA((n, 3)), pltpu.SemaphoreType.DMA((n, 3))],
    )(*parts)


def _share_halves(halves):
    n = len(halves)

    def body(*refs):
        ins, outs = refs[:n], refs[n:2 * n]
        send_sems, recv_sems, local_sems = refs[2 * n:]
        x, y, c = _position()
        remote, local = [], []
        for w in range(n):
            half = ins[w].shape[0]
            dst = outs[w].at[pl.ds(c * half, half), :]
            own = pltpu.make_async_copy(ins[w], dst, local_sems.at[w])
            own.start()
            local.append(own)
            cp = pltpu.make_async_remote_copy(src_ref=ins[w], dst_ref=dst, send_sem=send_sems.at[w],
                                              recv_sem=recv_sems.at[w], device_id=(x, y, 1 - c), device_id_type=MESH)
            cp.start()
            remote.append(cp)
        for w in range(n):
            half = ins[w].shape[0]
            theirs = outs[w].at[pl.ds((1 - c) * half, half), :]
            pltpu.make_async_remote_copy(src_ref=ins[w], dst_ref=theirs, send_sem=send_sems.at[w],
                                         recv_sem=recv_sems.at[w], device_id=(x, y, 1 - c), device_id_type=MESH).wait_recv()
        for cp in remote:
            cp.wait_send()
        for cp in local:
            cp.wait()

    return _pcall(
        body, name="share_halves", in_specs=[HBM_SPEC] * n, out_specs=[HBM_SPEC] * n,
        out_shape=[jax.ShapeDtypeStruct((2 * h.shape[0], h.shape[1]), h.dtype) for h in halves],
        scratch_shapes=[pltpu.SemaphoreType.DMA((n,)), pltpu.SemaphoreType.DMA((n,)), pltpu.SemaphoreType.DMA((n,))],
    )(*halves)


def _sum_small(part):
    rows, width = part.shape

    def body(x_ref, out_ref, all_ref, send_sems, recv_sems):
        x, y, c = _position()
        me, sibling = (x, y, c), (x, y, 1 - c)
        chips = _other_chips(x, y)

        def block(px, py, pc):
            return all_ref.at[pl.ds((4 * px + 2 * py + pc) * rows, rows), :]

        def copy(k, blk, to, src=None):
            return pltpu.make_async_remote_copy(
                src_ref=block(*blk) if src is None else src, dst_ref=block(*blk), send_sem=send_sems.at[k],
                recv_sem=recv_sems.at[k], device_id=to, device_id_type=MESH)

        all_ref[pl.ds((4 * x + 2 * y + c) * rows, rows), :] = x_ref[...]
        first = [copy(0, me, sibling, src=x_ref)]
        first += [copy(1 + j, me, (*chip, c), src=x_ref) for j, chip in enumerate(chips)]
        for cp in first:
            cp.start()
        passed = [copy(4 + j, (*chip, c), sibling) for j, chip in enumerate(chips)]
        for j, chip in enumerate(chips):
            copy(1 + j, (*chip, c), me).wait_recv()
            passed[j].start()
        copy(0, sibling, me).wait_recv()
        for j, chip in enumerate(chips):
            copy(4 + j, (*chip, 1 - c), me).wait_recv()
        for cp in first + passed:
            cp.wait_send()
        total = all_ref[0:rows, :]
        for d in range(1, 8):
            total = total + all_ref[d * rows:(d + 1) * rows, :]
        out_ref[...] = total

    vm = pl.BlockSpec(memory_space=pltpu.VMEM)
    return _pcall(
        body, name="sum_small", in_specs=[vm], out_specs=vm, out_shape=jax.ShapeDtypeStruct((rows, width), F32),
        scratch_shapes=[pltpu.VMEM((8 * rows, width), F32), pltpu.SemaphoreType.DMA((7,)), pltpu.SemaphoreType.DMA((7,))],
    )(part)


def _row_tile(R, C, itemsize=4, budget=1 << 20):
    for t in (512, 256, 128, 64, 32, 16, 8):
        if R % t == 0 and t * C * itemsize <= budget:
            return t
    return R


def _add_halves(g, recv, c, *, name):
    _, R, C = g.shape
    half = R // 2
    t = _row_tile(half, C)
    nb = half // t

    def body(c_ref, g_ref, r_ref, o_ref):
        o_ref[...] = (g_ref[...].astype(F32) + r_ref[...].astype(F32)).astype(o_ref.dtype)

    grid_spec = pltpu.PrefetchScalarGridSpec(
        num_scalar_prefetch=1, grid=(4, nb),
        in_specs=[pl.BlockSpec((1, t, C), lambda k, i, cr: (k, cr[0] * nb + i, 0)),
                  pl.BlockSpec((1, t, C), lambda k, i, cr: (k, i, 0))],
        out_specs=pl.BlockSpec((1, t, C), lambda k, i, cr: (k, i, 0)))
    return _pcall(body, name=name, grid_spec=grid_spec, out_shape=jax.ShapeDtypeStruct((4, half, C), g.dtype),
                  compiler_params=_params("parallel", "parallel"))(c, g, recv)


def _add_owners(mine, recv, *, name):
    half, C = mine.shape
    t = _row_tile(half, C)

    def body(m_ref, r_ref, o_ref):
        o_ref[...] = ((m_ref[...].astype(F32) + r_ref[0].astype(F32)) + r_ref[1].astype(F32)) + r_ref[2].astype(F32)

    return _pcall(body, name=name, grid=(half // t,),
                  in_specs=[pl.BlockSpec((t, C), lambda i: (i, 0)), pl.BlockSpec((3, t, C), lambda i: (0, i, 0))],
                  out_specs=pl.BlockSpec((t, C), lambda i: (i, 0)), out_shape=jax.ShapeDtypeStruct((half, C), F32),
                  compiler_params=_params("parallel"))(mine, recv)


def _adamw(w, g, m, v, *, name):
    R, C = w.shape
    t = _row_tile(R, C)
    c1 = 1.0 - ADAM_B1 ** ADAM_STEP
    c2 = 1.0 - ADAM_B2 ** ADAM_STEP

    def body(w_ref, g_ref, m_ref, v_ref, d_ref, nm_ref, nv_ref):
        gv = g_ref[...]
        mn = ADAM_B1 * m_ref[...] + (1.0 - ADAM_B1) * gv
        vn = ADAM_B2 * v_ref[...] + (1.0 - ADAM_B2) * (gv * gv)
        d_ref[...] = -ADAM_LR * ((mn / c1) / (jnp.sqrt(vn / c2) + ADAM_EPS) + ADAM_WD * w_ref[...])
        nm_ref[...] = mn
        nv_ref[...] = vn

    blk = pl.BlockSpec((t, C), lambda i: (i, 0))
    shp = jax.ShapeDtypeStruct((R, C), F32)
    return _pcall(body, name=name, grid=(R // t,), in_specs=[blk] * 4, out_specs=[blk] * 3, out_shape=[shp] * 3,
                  compiler_params=_params("parallel"))(w, g, m, v)


BIG = ("w_in", "w_dil_out", "w_fox_out", "w_out", "w_ffn_in", "w_ffn_down")
SMALL = ("norm_mix_g", "b_fgt", "b_gate", "norm_ffn_g", "norm_final_g")
ORDER = ("norm_mix_g", "w_in", "b_fgt", "b_gate", "w_dil_out", "w_fox_out", "w_out", "norm_ffn_g", "w_ffn_in",
         "w_ffn_down", "norm_final_g")
SMALL_ROWS = {"norm_mix_g": (0, 1), "b_gate": (1, 3), "norm_ffn_g": (3, 4), "norm_final_g": (4, 5), "b_fgt": (5, 6)}


def _columns_to_blocks(full, ncol):
    K = full.shape[0]
    return full.reshape(K, 4, ncol).transpose(1, 0, 2)


def _blocks_to_columns(blocks):
    n, K, ncol = blocks.shape
    return blocks.transpose(1, 0, 2).reshape(K, n * ncol)


def kernel(x, norm_mix_g, w_in, b_fgt, b_gate, w_dil_out, w_fox_out, w_out, norm_ffn_g, w_ffn_in, w_ffn_down, norm_final_g, loss_target, m_norm_mix_g, m_w_in, m_b_fgt, m_b_gate, m_w_dil_out, m_w_fox_out, m_w_out, m_norm_ffn_g, m_w_ffn_in, m_w_ffn_down, m_norm_final_g, v_norm_mix_g, v_w_in, v_b_fgt, v_b_gate, v_w_dil_out, v_w_fox_out, v_w_out, v_norm_ffn_g, v_w_ffn_in, v_w_ffn_down, v_norm_final_g):
    weights = dict(norm_mix_g=norm_mix_g, w_in=w_in, b_fgt=b_fgt, b_gate=b_gate, w_dil_out=w_dil_out,
                   w_fox_out=w_fox_out, w_out=w_out, norm_ffn_g=norm_ffn_g, w_ffn_in=w_ffn_in, w_ffn_down=w_ffn_down,
                   norm_final_g=norm_final_g)
    m_in = dict(norm_mix_g=m_norm_mix_g, w_in=m_w_in, b_fgt=m_b_fgt, b_gate=m_b_gate, w_dil_out=m_w_dil_out,
                w_fox_out=m_w_fox_out, w_out=m_w_out, norm_ffn_g=m_norm_ffn_g, w_ffn_in=m_w_ffn_in,
                w_ffn_down=m_w_ffn_down, norm_final_g=m_norm_final_g)
    v_in = dict(norm_mix_g=v_norm_mix_g, w_in=v_w_in, b_fgt=v_b_fgt, b_gate=v_b_gate, w_dil_out=v_w_dil_out,
                w_fox_out=v_w_fox_out, w_out=v_w_out, norm_ffn_g=v_norm_ffn_g, w_ffn_in=v_w_ffn_in,
                w_ffn_down=v_w_ffn_down, norm_final_g=v_norm_final_g)
    c = lax.axis_index("c")
    chip = 2 * lax.axis_index("x") + lax.axis_index("y")

    shards = [weights[n][0].astype(_CD) for n in BIG]
    g_in, g_dil, g_fox, g_out, g_ffn_in, g_ffn_down = _gather_weights(shards)
    full_in = _blocks_to_columns(g_in)
    o3 = QKV_COLS
    o4 = o3 + N_FOX_HEADS
    w = dict(
        qkv=full_in[:, :o3],
        f=jnp.pad(full_in[:, o3:o4], ((0, 0), (0, F_PAD - N_FOX_HEADS))),
        g=full_in[:, o4:],
        dil_out=_blocks_to_columns(g_dil), fox_out=_blocks_to_columns(g_fox),
        out=g_out.reshape(D_MODEL, D_MODEL), ffn_in=_blocks_to_columns(g_ffn_in),
        ffn_down=g_ffn_down.reshape(D_FF, D_MODEL))
    p = dict(norm_mix_g=norm_mix_g, b_fgt=jnp.pad(b_fgt, ((0, 0), (0, F_PAD - N_FOX_HEADS))), b_gate=b_gate,
             norm_ffn_g=norm_ffn_g, norm_final_g=norm_final_g.reshape(1, D_MODEL))

    loss_part, grad_x, gw, small = _layer_step(x[0], loss_target[0], w, p)

    gw_in = jnp.concatenate([gw["qkv"], gw["f"][:, :N_FOX_HEADS], gw["g"]], axis=1)
    blocks = [
        _columns_to_blocks(gw_in, w_in.shape[2]),
        _columns_to_blocks(gw["dil_out"], w_dil_out.shape[2]),
        _columns_to_blocks(gw["fox_out"], w_fox_out.shape[2]),
        gw["out"].reshape(4, w_out.shape[1], D_MODEL),
        _columns_to_blocks(gw["ffn_in"], w_ffn_in.shape[2]),
        gw["ffn_down"].reshape(4, w_ffn_down.shape[1], D_MODEL),
    ]
    from_sibling = _swap_halves(blocks)
    c_arr = jnp.reshape(c, (1,)).astype(jnp.int32)
    chip_sums = [_add_halves(b, r, c_arr, name=f"add_halves_{n}") for b, r, n in zip(blocks, from_sibling, BIG)]
    from_chips = _scatter_to_owners(chip_sums)
    halves = [_add_owners(lax.dynamic_index_in_dim(s, chip, 0, keepdims=False), r, name=f"add_owners_{n}")
              for s, r, n in zip(chip_sums, from_chips, BIG)]
    grads = dict(zip(BIG, _share_halves(halves)))

    packed = jnp.concatenate([
        small["norm_mix_g"], small["b_gate"].reshape(2, D_MODEL), small["norm_ffn_g"], small["norm_final_g"],
        jnp.pad(small["b_fgt"], ((0, 0), (0, D_MODEL - F_PAD))), jnp.zeros((2, D_MODEL), F32)], axis=0)
    summed = _sum_small(packed)
    for n in SMALL:
        lo, hi = SMALL_ROWS[n]
        grads[n] = summed[lo:hi].reshape(1, -1)[:, :weights[n].size]

    loss = lax.psum(loss_part[0, 0], ("x", "y", "c"))

    out_g, out_d, out_m, out_v = {}, {}, {}, {}
    for n in ORDER:
        shape = weights[n].shape
        two_d = shape[1:] if len(shape) == 3 else (1, weights[n].size)
        g2 = grads[n].reshape(two_d)
        d2, m2, v2 = _adamw(weights[n].reshape(two_d), g2, m_in[n].reshape(two_d), v_in[n].reshape(two_d),
                            name=f"adamw_{n}")
        out_g[n], out_d[n], out_m[n], out_v[n] = (g2.reshape(shape), d2.reshape(shape), m2.reshape(shape),
                                                  v2.reshape(shape))
    return (loss, grad_x[None], *[out_g[n] for n in ORDER], *[out_d[n] for n in ORDER],
            *[out_m[n] for n in ORDER], *[out_v[n] for n in ORDER])
```

```python
import numpy as np
import jax
import jax.numpy as jnp
from jax import lax
from jax.experimental import pallas as pl
from jax.experimental.pallas import tpu as pltpu

F32 = jnp.float32
_CD = jnp.bfloat16

D_MODEL = 1024
HEAD_DIM = 64
DIL_PAIRS = ((128, 1), (512, 4), (2048, 16))
N_DIL_GROUPS = 3
DIL_HEADS = 4
DIL_W = 128
DIL_OUT = DIL_HEADS * HEAD_DIM
DIL_WIDTH = N_DIL_GROUPS * DIL_OUT
N_FOX_HEADS = 8
FOX_WIDTH = N_FOX_HEADS * HEAD_DIM
D_FF = 2816
QKV_COLS = 3 * DIL_WIDTH + 3 * FOX_WIDTH
F_PAD = 128
RMS_EPS = 1e-6
NEG_INF = -1e30
ATTN_SCALE = HEAD_DIM ** -0.5
ADAM_LR, ADAM_B1, ADAM_B2, ADAM_EPS, ADAM_WD, ADAM_STEP = 0.001, 0.9, 0.999, 1e-08, 0.01, 10

VMEM_LIMIT = 48 * 1024 * 1024
MESH = pl.DeviceIdType.MESH
HBM_SPEC = pl.BlockSpec(memory_space=pltpu.HBM)


def _pcall(body, **kw):
    return pl.pallas_call(body, **kw)


def _params(*sem):
    return pltpu.CompilerParams(dimension_semantics=sem, vmem_limit_bytes=VMEM_LIMIT)


def _pick(dim, pref):
    t = (min(pref, dim) // 128) * 128
    while t >= 128:
        if dim % t == 0:
            return t
        t -= 128
    return dim


def _mm(a, b, *, name, ta=False, tb=False, out_dtype=F32, add=None, tm=512, tn=512, tk=2048):
    if ta:
        K, M = a.shape
    else:
        M, K = a.shape
    if tb:
        N, K2 = b.shape
    else:
        K2, N = b.shape
    assert K == K2, (a.shape, b.shape)
    tm, tn, tk = _pick(M, tm), _pick(N, tn), _pick(K, tk)
    nk = K // tk
    dn = (((0 if ta else 1,), (1 if tb else 0,)), ((), ()))
    has_add = add is not None

    def body(*refs):
        a_ref, b_ref = refs[0], refs[1]
        add_ref = refs[2] if has_add else None
        o_ref = refs[3] if has_add else refs[2]
        p = lax.dot_general(a_ref[...].astype(_CD), b_ref[...].astype(_CD), dn, preferred_element_type=F32)

        def finish(r):
            if has_add:
                r = r + add_ref[...]
            o_ref[...] = r.astype(out_dtype)

        if nk == 1:
            finish(p)
        else:
            acc_ref = refs[-1]
            k = pl.program_id(2)

            @pl.when(k == 0)
            def _():
                acc_ref[...] = p

            @pl.when(k > 0)
            def _():
                acc_ref[...] += p

            @pl.when(k == nk - 1)
            def _():
                finish(acc_ref[...])

    a_spec = pl.BlockSpec((tk, tm), lambda i, j, k: (k, i)) if ta else pl.BlockSpec((tm, tk), lambda i, j, k: (i, k))
    b_spec = pl.BlockSpec((tn, tk), lambda i, j, k: (j, k)) if tb else pl.BlockSpec((tk, tn), lambda i, j, k: (k, j))
    o_spec = pl.BlockSpec((tm, tn), lambda i, j, k: (i, j))
    in_specs = [a_spec, b_spec] + ([o_spec] if has_add else [])
    args = (a, b) + ((add,) if has_add else ())
    return _pcall(
        body, name=name, grid=(M // tm, N // tn, nk), in_specs=in_specs, out_specs=o_spec,
        out_shape=jax.ShapeDtypeStruct((M, N), out_dtype),
        scratch_shapes=[pltpu.VMEM((tm, tn), F32)] if nk > 1 else [],
        compiler_params=_params("parallel", "parallel", "arbitrary"),
    )(*args)


def _rms_fwd(x, g, *, name, tm=512):
    S, D = x.shape

    def body(x_ref, g_ref, h_ref):
        xv = x_ref[...]
        r = lax.rsqrt(jnp.mean(xv * xv, axis=-1, keepdims=True) + RMS_EPS)
        h_ref[...] = ((xv * r) * g_ref[...]).astype(h_ref.dtype)

    row = pl.BlockSpec((tm, D), lambda i: (i, 0))
    return _pcall(body, name=name, grid=(S // tm,), in_specs=[row, pl.BlockSpec((1, D), lambda i: (0, 0))],
                  out_specs=row, out_shape=jax.ShapeDtypeStruct((S, D), _CD), compiler_params=_params("parallel"))(x, g)


def _rms_bwd(x, g, dh, dres, *, name, tm=512):
    S, D = x.shape

    def body(x_ref, g_ref, dh_ref, dres_ref, dx_ref, dg_ref):
        xv = x_ref[...]
        r = lax.rsqrt(jnp.mean(xv * xv, axis=-1, keepdims=True) + RMS_EPS)
        xh = xv * r
        dhv = dh_ref[...]
        dxh = dhv * g_ref[...]
        dx_ref[...] = dres_ref[...] + r * (dxh - xh * jnp.mean(dxh * xh, axis=-1, keepdims=True))
        part = jnp.sum(dhv * xh, axis=0, keepdims=True)

        @pl.when(pl.program_id(0) == 0)
        def _():
            dg_ref[...] = part

        @pl.when(pl.program_id(0) > 0)
        def _():
            dg_ref[...] += part

    row = pl.BlockSpec((tm, D), lambda i: (i, 0))
    vec = pl.BlockSpec((1, D), lambda i: (0, 0))
    return _pcall(body, name=name, grid=(S // tm,), in_specs=[row, vec, row, row], out_specs=[row, vec],
                  out_shape=[jax.ShapeDtypeStruct((S, D), F32), jax.ShapeDtypeStruct((1, D), F32)],
                  compiler_params=_params("arbitrary"))(x, g, dh, dres)


def _loss_head(x, g, tgt, *, name, tm=512):
    S, D = x.shape

    def body(x_ref, g_ref, t_ref, loss_ref, dx_ref, dg_ref):
        xv = x_ref[...]
        gv = g_ref[...]
        r = lax.rsqrt(jnp.mean(xv * xv, axis=-1, keepdims=True) + RMS_EPS)
        xh = xv * r
        err = xh * gv - t_ref[...]
        lpart = 0.5 * jnp.sum(jnp.mean(err * err, axis=-1, keepdims=True), axis=0, keepdims=True)
        dy = err * (1.0 / D)
        dxh = dy * gv
        dx_ref[...] = r * (dxh - xh * jnp.mean(dxh * xh, axis=-1, keepdims=True))
        gpart = jnp.sum(dy * xh, axis=0, keepdims=True)

        @pl.when(pl.program_id(0) == 0)
        def _():
            loss_ref[...] = lpart
            dg_ref[...] = gpart

        @pl.when(pl.program_id(0) > 0)
        def _():
            loss_ref[...] += lpart
            dg_ref[...] += gpart

    row = pl.BlockSpec((tm, D), lambda i: (i, 0))
    vec = pl.BlockSpec((1, D), lambda i: (0, 0))
    one = pl.BlockSpec((1, 1), lambda i: (0, 0))
    return _pcall(body, name=name, grid=(S // tm,), in_specs=[row, vec, row], out_specs=[one, row, vec],
                  out_shape=[jax.ShapeDtypeStruct((1, 1), F32), jax.ShapeDtypeStruct((S, D), F32),
                             jax.ShapeDtypeStruct((1, D), F32)],
                  compiler_params=_params("arbitrary"))(x, g, tgt)


def _sigmoid(z):
    return 1.0 / (1.0 + jnp.exp(-z))


def _gate_fwd(gl, bg, ya, yb, *, name, tm=512):
    S, D = ya.shape

    def body(za_ref, zb_ref, ba_ref, bb_ref, ya_ref, yb_ref, o_ref):
        ga = _sigmoid(za_ref[...] + ba_ref[...])
        gb = _sigmoid(zb_ref[...] + bb_ref[...])
        o_ref[...] = (ga * ya_ref[...] + gb * yb_ref[...]).astype(o_ref.dtype)

    lo = pl.BlockSpec((tm, D), lambda i: (i, 0))
    hi = pl.BlockSpec((tm, D), lambda i: (i, 1))
    vlo = pl.BlockSpec((1, D), lambda i: (0, 0))
    vhi = pl.BlockSpec((1, D), lambda i: (0, 1))
    return _pcall(body, name=name, grid=(S // tm,), in_specs=[lo, hi, vlo, vhi, lo, lo], out_specs=lo,
                  out_shape=jax.ShapeDtypeStruct((S, D), _CD), compiler_params=_params("parallel"))(gl, gl, bg, bg, ya, yb)


def _gate_bwd(dm, gl, bg, ya, yb, *, name, tm=256):
    S, D = ya.shape

    def body(dm_ref, za_ref, zb_ref, ba_ref, bb_ref, ya_ref, yb_ref, dya_ref, dyb_ref, dgl_ref, dbg_ref):
        dmv = dm_ref[...]
        ga = _sigmoid(za_ref[...] + ba_ref[...])
        gb = _sigmoid(zb_ref[...] + bb_ref[...])
        dya_ref[...] = (dmv * ga).astype(dya_ref.dtype)
        dyb_ref[...] = (dmv * gb).astype(dyb_ref.dtype)
        dza = dmv * ya_ref[...] * ga * (1.0 - ga)
        dzb = dmv * yb_ref[...] * gb * (1.0 - gb)
        dgl_ref[:, :D] = dza.astype(dgl_ref.dtype)
        dgl_ref[:, D:] = dzb.astype(dgl_ref.dtype)
        pa = jnp.sum(dza, axis=0, keepdims=True)
        pb = jnp.sum(dzb, axis=0, keepdims=True)

        @pl.when(pl.program_id(0) == 0)
        def _():
            dbg_ref[:, :D] = pa
            dbg_ref[:, D:] = pb

        @pl.when(pl.program_id(0) > 0)
        def _():
            dbg_ref[:, :D] += pa
            dbg_ref[:, D:] += pb

    lo = pl.BlockSpec((tm, D), lambda i: (i, 0))
    hi = pl.BlockSpec((tm, D), lambda i: (i, 1))
    vlo = pl.BlockSpec((1, D), lambda i: (0, 0))
    vhi = pl.BlockSpec((1, D), lambda i: (0, 1))
    wide = pl.BlockSpec((tm, 2 * D), lambda i: (i, 0))
    vwide = pl.BlockSpec((1, 2 * D), lambda i: (0, 0))
    return _pcall(body, name=name, grid=(S // tm,), in_specs=[lo, lo, hi, vlo, vhi, lo, lo],
                  out_specs=[lo, lo, wide, vwide],
                  out_shape=[jax.ShapeDtypeStruct((S, D), _CD), jax.ShapeDtypeStruct((S, D), _CD),
                             jax.ShapeDtypeStruct((S, 2 * D), _CD), jax.ShapeDtypeStruct((1, 2 * D), F32)],
                  compiler_params=_params("arbitrary"))(dm, gl, gl, bg, bg, ya, yb)


def _swiglu_fwd(gu, *, name, tm=256):
    S, F2 = gu.shape
    F = F2 // 2

    def body(g_ref, u_ref, o_ref):
        gv = g_ref[...]
        o_ref[...] = (gv * _sigmoid(gv) * u_ref[...]).astype(o_ref.dtype)

    lo = pl.BlockSpec((tm, F), lambda i: (i, 0))
    hi = pl.BlockSpec((tm, F), lambda i: (i, 1))
    return _pcall(body, name=name, grid=(S // tm,), in_specs=[lo, hi], out_specs=lo,
                  out_shape=jax.ShapeDtypeStruct((S, F), _CD), compiler_params=_params("parallel"))(gu, gu)


def _swiglu_bwd(dact, gu, *, name, tm=256):
    S, F2 = gu.shape
    F = F2 // 2

    def body(d_ref, g_ref, u_ref, o_ref):
        dv = d_ref[...]
        gv = g_ref[...]
        sg = _sigmoid(gv)
        o_ref[:, :F] = (dv * u_ref[...] * (sg * (1.0 + gv * (1.0 - sg)))).astype(o_ref.dtype)
        o_ref[:, F:] = (dv * (gv * sg)).astype(o_ref.dtype)

    lo = pl.BlockSpec((tm, F), lambda i: (i, 0))
    hi = pl.BlockSpec((tm, F), lambda i: (i, 1))
    return _pcall(body, name=name, grid=(S // tm,), in_specs=[lo, lo, hi],
                  out_specs=pl.BlockSpec((tm, F2), lambda i: (i, 0)),
                  out_shape=jax.ShapeDtypeStruct((S, F2), _CD), compiler_params=_params("parallel"))(dact, gu, gu)


def _split3(x):
    hi = x.astype(jnp.bfloat16)
    r1 = x - hi.astype(F32)
    mid = r1.astype(jnp.bfloat16)
    lo = (r1 - mid.astype(F32)).astype(jnp.bfloat16)
    return hi, mid, lo


def _ones_dot_left(ones, x):
    return sum(jnp.dot(ones, p, preferred_element_type=F32) for p in _split3(x))


def _ones_dot_right(x, ones):
    return sum(jnp.dot(p, ones, preferred_element_type=F32) for p in _split3(x))


def _head_sum(x):
    n = x.shape[1]
    r = lax.broadcasted_iota(jnp.int32, (n, n), 0) // HEAD_DIM
    c = lax.broadcasted_iota(jnp.int32, (n, n), 1) // HEAD_DIM
    return _ones_dot_right(x, (r == c).astype(jnp.bfloat16))


def _log_sigmoid(z):
    e = jnp.exp(-jnp.abs(z))
    t = 1.0 + e
    log1p_e = jnp.where(t == 1.0, e, jnp.log(t) * (e / jnp.where(t == 1.0, 1.0, t - 1.0)))
    return jnp.minimum(z, 0.0) - log1p_e


def _fox_cumsum(zf, bf, *, name):
    S, W = zf.shape
    nb = S // 128

    def body(z_ref, b_ref, c_ref):
        tri = (lax.broadcasted_iota(jnp.int32, (128, 128), 0) >= lax.broadcasted_iota(jnp.int32, (128, 128), 1))
        tri = tri.astype(jnp.bfloat16)

        def step(i, carry):
            rows = pl.ds(pl.multiple_of(i * 128, 128), 128)
            lf = _log_sigmoid(z_ref[rows, :] + b_ref[...])
            cb = _ones_dot_left(tri, lf) + carry
            c_ref[rows, :] = cb
            return cb[127:128, :]

        lax.fori_loop(0, nb, step, jnp.zeros((1, W), F32))

    return _pcall(body, name=name, out_shape=jax.ShapeDtypeStruct((S, W), F32),
                  compiler_params=pltpu.CompilerParams(vmem_limit_bytes=VMEM_LIMIT))(zf, bf)


def _fox_cumsum_bwd(dc, zf, bf, *, name):
    S, W = zf.shape
    nb = S // 128

    def body(dc_ref, z_ref, b_ref, dz_ref, db_ref):
        tri = (lax.broadcasted_iota(jnp.int32, (128, 128), 0) <= lax.broadcasted_iota(jnp.int32, (128, 128), 1))
        tri = tri.astype(jnp.bfloat16)

        def step(k, carry):
            tail, acc = carry
            i = nb - 1 - k
            rows = pl.ds(pl.multiple_of(i * 128, 128), 128)
            dlf = _ones_dot_left(tri, dc_ref[rows, :]) + tail
            dz = dlf * _sigmoid(-(z_ref[rows, :] + b_ref[...]))
            dz_ref[rows, :] = dz
            return dlf[0:1, :], acc + jnp.sum(dz, axis=0, keepdims=True)

        _, acc = lax.fori_loop(0, nb, step, (jnp.zeros((1, W), F32), jnp.zeros((1, W), F32)))
        db_ref[...] = acc

    return _pcall(body, name=name,
                  out_shape=[jax.ShapeDtypeStruct((S, W), F32), jax.ShapeDtypeStruct((1, W), F32)],
                  compiler_params=pltpu.CompilerParams(vmem_limit_bytes=VMEM_LIMIT))(dc, zf, bf)


def _dil_slopes(group):
    h = np.arange(1, N_DIL_GROUPS * DIL_HEADS + 1, dtype=np.float32)
    s = (np.float32(2.0) ** (np.float32(-8.0) * h / np.float32(N_DIL_GROUPS * DIL_HEADS))).astype(np.float32)
    return [float(v) for v in s.reshape(N_DIL_GROUPS, DIL_HEADS)[group]]


def _dil_tiles(i, n, blocks_per_seq):
    qi = lax.broadcasted_iota(jnp.int32, (DIL_W, DIL_W), 0)
    kj = lax.broadcasted_iota(jnp.int32, (DIL_W, DIL_W), 1)
    first = ((4 * n + i) % blocks_per_seq) == 0
    valid_prev = jnp.logical_and(kj >= qi, jnp.logical_not(first))
    valid_cur = kj <= qi
    rel_prev = (qi - kj + DIL_W).astype(F32)
    rel_cur = (qi - kj).astype(F32)
    return valid_prev, valid_cur, rel_prev, rel_cur


CHUNK = 4 * DIL_W


def _dil_fwd(q, k, v, group, *, name):
    S = q.shape[0]
    dilation = DIL_PAIRS[group][1]
    bps = (S // dilation) // DIL_W
    slopes = _dil_slopes(group)
    nt = (((1,), (1,)), ((), ()))

    def body(q_ref, k_ref, v_ref, kp_ref, vp_ref, o_ref, l_ref):
        n = pl.program_id(0)
        for i in range(4):
            valid_prev, valid_cur, rel_prev, rel_cur = _dil_tiles(i, n, bps)
            rows = slice(i * DIL_W, (i + 1) * DIL_W)
            prow = slice((i - 1) * DIL_W, i * DIL_W)
            for h in range(DIL_HEADS):
                cols = slice(h * HEAD_DIM, (h + 1) * HEAD_DIM)
                qh = q_ref[rows, cols]
                kc, vc = k_ref[rows, cols], v_ref[rows, cols]
                kp = kp_ref[:, cols] if i == 0 else k_ref[prow, cols]
                vp = vp_ref[:, cols] if i == 0 else v_ref[prow, cols]
                sl = slopes[h] * dilation
                sp = lax.dot_general(qh, kp, nt, preferred_element_type=F32) * ATTN_SCALE - sl * rel_prev
                sc = lax.dot_general(qh, kc, nt, preferred_element_type=F32) * ATTN_SCALE - sl * rel_cur
                sp = jnp.where(valid_prev, sp, NEG_INF)
                sc = jnp.where(valid_cur, sc, NEG_INF)
                m = jnp.maximum(jnp.max(sp, axis=-1, keepdims=True), jnp.max(sc, axis=-1, keepdims=True))
                pp, pc = jnp.exp(sp - m), jnp.exp(sc - m)
                den = jnp.sum(pp, axis=-1, keepdims=True) + jnp.sum(pc, axis=-1, keepdims=True)
                acc = (jnp.dot(pp.astype(_CD), vp, preferred_element_type=F32)
                       + jnp.dot(pc.astype(_CD), vc, preferred_element_type=F32))
                o_ref[rows, cols] = acc / den
                l_ref[rows, cols] = jnp.broadcast_to(m + jnp.log(den), (DIL_W, HEAD_DIM))

    cur = pl.BlockSpec((CHUNK, DIL_OUT), lambda n: (n, 0))
    prev = pl.BlockSpec((DIL_W, DIL_OUT), lambda n: (jnp.maximum(4 * n - 1, 0), 0))
    return _pcall(body, name=name, grid=(S // CHUNK,), in_specs=[cur, cur, cur, prev, prev], out_specs=[cur, cur],
                  out_shape=[jax.ShapeDtypeStruct((S, DIL_OUT), F32), jax.ShapeDtypeStruct((S, DIL_OUT), F32)],
                  compiler_params=_params("parallel"))(q, k, v, k, v)


def _dil_bwd(q, k, v, o, lse, do, dlse, group, *, name):
    S = q.shape[0]
    dilation = DIL_PAIRS[group][1]
    bps = (S // dilation) // DIL_W
    slopes = _dil_slopes(group)
    nchunk = S // CHUNK
    nt = (((1,), (1,)), ((), ()))
    tn = (((0,), (0,)), ((), ()))

    def body(q_ref, k_ref, v_ref, kp_ref, vp_ref, o_ref, l_ref, do_ref, dl_ref, dq_ref, dk_ref, dv_ref,
             dk_s, dv_s):
        step = pl.program_id(0)
        n = nchunk - 1 - step

        @pl.when(step == 0)
        def _():
            dk_s[CHUNK:, :] = jnp.zeros((DIL_W, DIL_OUT), F32)
            dv_s[CHUNK:, :] = jnp.zeros((DIL_W, DIL_OUT), F32)

        dk_s[:CHUNK, :] = jnp.zeros((CHUNK, DIL_OUT), F32)
        dv_s[:CHUNK, :] = jnp.zeros((CHUNK, DIL_OUT), F32)
        for i in range(4):
            valid_prev, valid_cur, rel_prev, rel_cur = _dil_tiles(i, n, bps)
            rows = slice(i * DIL_W, (i + 1) * DIL_W)
            prow = slice((i - 1) * DIL_W, i * DIL_W)
            s_prev = slice(i * DIL_W, (i + 1) * DIL_W)
            s_cur = slice((i + 1) * DIL_W, (i + 2) * DIL_W)
            for h in range(DIL_HEADS):
                cols = slice(h * HEAD_DIM, (h + 1) * HEAD_DIM)
                qh = q_ref[rows, cols]
                kc, vc = k_ref[rows, cols], v_ref[rows, cols]
                kp = kp_ref[:, cols] if i == 0 else k_ref[prow, cols]
                vp = vp_ref[:, cols] if i == 0 else v_ref[prow, cols]
                sl = slopes[h] * dilation
                lh = l_ref[rows, h * HEAD_DIM:h * HEAD_DIM + 1]
                sp = lax.dot_general(qh, kp, nt, preferred_element_type=F32) * ATTN_SCALE - sl * rel_prev
                sc = lax.dot_general(qh, kc, nt, preferred_element_type=F32) * ATTN_SCALE - sl * rel_cur
                pp = jnp.exp(jnp.where(valid_prev, sp, NEG_INF) - lh)
                pc = jnp.exp(jnp.where(valid_cur, sc, NEG_INF) - lh)
                doh = do_ref[rows, cols]
                dsum = jnp.sum(doh * o_ref[rows, cols], axis=-1, keepdims=True)
                shift = dl_ref[rows, h * HEAD_DIM:h * HEAD_DIM + 1] - dsum
                dob = doh.astype(_CD)
                dsp = pp * (lax.dot_general(dob, vp, nt, preferred_element_type=F32) + shift)
                dsc = pc * (lax.dot_general(dob, vc, nt, preferred_element_type=F32) + shift)
                dspb = (dsp * ATTN_SCALE).astype(_CD)
                dscb = (dsc * ATTN_SCALE).astype(_CD)
                dq_ref[rows, cols] = (jnp.dot(dspb, kp, preferred_element_type=F32)
                                      + jnp.dot(dscb, kc, preferred_element_type=F32)).astype(dq_ref.dtype)
                dk_s[s_prev, cols] += lax.dot_general(dspb, qh, tn, preferred_element_type=F32)
                dk_s[s_cur, cols] += lax.dot_general(dscb, qh, tn, preferred_element_type=F32)
                dv_s[s_prev, cols] += lax.dot_general(pp.astype(_CD), dob, tn, preferred_element_type=F32)
                dv_s[s_cur, cols] += lax.dot_general(pc.astype(_CD), dob, tn, preferred_element_type=F32)
        dk_ref[...] = dk_s[DIL_W:, :].astype(dk_ref.dtype)
        dv_ref[...] = dv_s[DIL_W:, :].astype(dv_ref.dtype)
        dk_s[CHUNK:, :] = dk_s[:DIL_W, :]
        dv_s[CHUNK:, :] = dv_s[:DIL_W, :]

    cur = pl.BlockSpec((CHUNK, DIL_OUT), lambda s: (nchunk - 1 - s, 0))
    prev = pl.BlockSpec((DIL_W, DIL_OUT), lambda s: (jnp.maximum(4 * (nchunk - 1 - s) - 1, 0), 0))
    shp = jax.ShapeDtypeStruct((S, DIL_OUT), _CD)
    return _pcall(body, name=name, grid=(nchunk,), in_specs=[cur, cur, cur, prev, prev, cur, cur, cur, cur],
                  out_specs=[cur, cur, cur], out_shape=[shp, shp, shp],
                  scratch_shapes=[pltpu.VMEM((CHUNK + DIL_W, DIL_OUT), F32), pltpu.VMEM((CHUNK + DIL_W, DIL_OUT), F32)],
                  compiler_params=_params("arbitrary"))(q, k, v, k, v, o, lse, do, dlse)


def _dil_mix_fwd(os_, ls_, *, name, tm=512):
    S, W = os_[0].shape

    def body(o0, o1, o2, l0, l1, l2, out_ref):
        ls = [l0[...], l1[...], l2[...]]
        m = jnp.maximum(jnp.maximum(ls[0], ls[1]), ls[2])
        es = [jnp.exp(l - m) for l in ls]
        den = es[0] + es[1] + es[2]
        out_ref[...] = ((es[0] * o0[...] + es[1] * o1[...] + es[2] * o2[...]) / den).astype(out_ref.dtype)

    row = pl.BlockSpec((tm, W), lambda i: (i, 0))
    return _pcall(body, name=name, grid=(S // tm,), in_specs=[row] * 6, out_specs=row,
                  out_shape=jax.ShapeDtypeStruct((S, W), _CD), compiler_params=_params("parallel"))(*os_, *ls_)


def _dil_mix_bwd(doa, os_, ls_, *, name, tm=512):
    S, W = doa.shape

    def body(d_ref, o0, o1, o2, l0, l1, l2, do0, do1, do2, dl0, dl1, dl2):
        dv = d_ref[...]
        ls = [l0[...], l1[...], l2[...]]
        m = jnp.maximum(jnp.maximum(ls[0], ls[1]), ls[2])
        es = [jnp.exp(l - m) for l in ls]
        den = es[0] + es[1] + es[2]
        al = [e / den for e in es]
        da = [_head_sum(dv * o[...]) for o in (o0, o1, o2)]
        mean = al[0] * da[0] + al[1] * da[1] + al[2] * da[2]
        for a, d_, do_ref, dl_ref in zip(al, da, (do0, do1, do2), (dl0, dl1, dl2)):
            do_ref[...] = a * dv
            dl_ref[...] = a * (d_ - mean)

    row = pl.BlockSpec((tm, W), lambda i: (i, 0))
    shp = jax.ShapeDtypeStruct((S, W), F32)
    return _pcall(body, name=name, grid=(S // tm,), in_specs=[row] * 7, out_specs=[row] * 6, out_shape=[shp] * 6,
                  compiler_params=_params("parallel"))(doa, *os_, *ls_)


FOX_T = 512


def _fox_scores(q, k, cq, ck, i, j):
    s = lax.dot_general(q, k, (((1,), (1,)), ((), ())), preferred_element_type=F32) * ATTN_SCALE + (cq - ck)
    qpos = i * FOX_T + lax.broadcasted_iota(jnp.int32, (FOX_T, FOX_T), 0)
    kpos = j * FOX_T + lax.broadcasted_iota(jnp.int32, (FOX_T, FOX_T), 1)
    return jnp.where(kpos <= qpos, s, NEG_INF)


def _fox_fwd(q, k, v, cq, ck, *, name):
    H, S, dh = q.shape
    nt = S // FOX_T

    def body(q_ref, k_ref, v_ref, cq_ref, ck_ref, o_ref, l_ref, m_s, l_s, acc_s):
        i, j = pl.program_id(1), pl.program_id(2)

        @pl.when(j == 0)
        def _():
            m_s[...] = jnp.full((FOX_T, 1), NEG_INF, F32)
            l_s[...] = jnp.zeros((FOX_T, 1), F32)
            acc_s[...] = jnp.zeros((FOX_T, dh), F32)

        @pl.when(j <= i)
        def _():
            s = _fox_scores(q_ref[0], k_ref[0], cq_ref[0], ck_ref[0], i, j)
            m_new = jnp.maximum(m_s[...], jnp.max(s, axis=-1, keepdims=True))
            alpha = jnp.exp(m_s[...] - m_new)
            p = jnp.exp(s - m_new)
            l_s[...] = alpha * l_s[...] + jnp.sum(p, axis=-1, keepdims=True)
            acc_s[...] = alpha * acc_s[...] + jnp.dot(p.astype(_CD), v_ref[0], preferred_element_type=F32)
            m_s[...] = m_new

        @pl.when(j == nt - 1)
        def _():
            o_ref[0] = acc_s[...] / l_s[...]
            l_ref[0] = m_s[...] + jnp.log(l_s[...])

    qs = pl.BlockSpec((1, FOX_T, dh), lambda h, i, j: (h, i, 0))
    ks = pl.BlockSpec((1, FOX_T, dh), lambda h, i, j: (h, jnp.minimum(i, j), 0))
    cqs = pl.BlockSpec((1, FOX_T, 1), lambda h, i, j: (h, i, 0))
    cks = pl.BlockSpec((1, 1, FOX_T), lambda h, i, j: (h, 0, jnp.minimum(i, j)))
    return _pcall(body, name=name, grid=(H, nt, nt), in_specs=[qs, ks, ks, cqs, cks], out_specs=[qs, cqs],
                  out_shape=[jax.ShapeDtypeStruct((H, S, dh), F32), jax.ShapeDtypeStruct((H, S, 1), F32)],
                  scratch_shapes=[pltpu.VMEM((FOX_T, 1), F32), pltpu.VMEM((FOX_T, 1), F32), pltpu.VMEM((FOX_T, dh), F32)],
                  compiler_params=_params("parallel", "parallel", "arbitrary"))(q, k, v, cq, ck)


def _fox_bwd_q(q, k, v, cq, ck, o, lse, do, *, name):
    H, S, dh = q.shape
    nt = S // FOX_T

    def body(q_ref, k_ref, v_ref, cq_ref, ck_ref, o_ref, l_ref, do_ref, dq_ref, dcq_ref, ds_ref, dq_s, dc_s):
        i, j = pl.program_id(1), pl.program_id(2)

        @pl.when(j == 0)
        def _():
            dq_s[...] = jnp.zeros((FOX_T, dh), F32)
            dc_s[...] = jnp.zeros((FOX_T, 1), F32)
            ds_ref[0] = jnp.sum(do_ref[0] * o_ref[0], axis=-1, keepdims=True)

        @pl.when(j <= i)
        def _():
            s = _fox_scores(q_ref[0], k_ref[0], cq_ref[0], ck_ref[0], i, j)
            p = jnp.exp(s - l_ref[0])
            dp = lax.dot_general(do_ref[0].astype(_CD), v_ref[0], (((1,), (1,)), ((), ())), preferred_element_type=F32)
            ds = p * (dp - ds_ref[0])
            dc_s[...] += jnp.sum(ds, axis=-1, keepdims=True)
            dq_s[...] += jnp.dot((ds * ATTN_SCALE).astype(_CD), k_ref[0], preferred_element_type=F32)

        @pl.when(j == nt - 1)
        def _():
            dq_ref[0] = dq_s[...].astype(dq_ref.dtype)
            dcq_ref[0] = dc_s[...]

    qs = pl.BlockSpec((1, FOX_T, dh), lambda h, i, j: (h, i, 0))
    ks = pl.BlockSpec((1, FOX_T, dh), lambda h, i, j: (h, jnp.minimum(i, j), 0))
    cqs = pl.BlockSpec((1, FOX_T, 1), lambda h, i, j: (h, i, 0))
    cks = pl.BlockSpec((1, 1, FOX_T), lambda h, i, j: (h, 0, jnp.minimum(i, j)))
    col = jax.ShapeDtypeStruct((H, S, 1), F32)
    return _pcall(body, name=name, grid=(H, nt, nt), in_specs=[qs, ks, ks, cqs, cks, qs, cqs, qs],
                  out_specs=[qs, cqs, cqs], out_shape=[jax.ShapeDtypeStruct((H, S, dh), _CD), col, col],
                  scratch_shapes=[pltpu.VMEM((FOX_T, dh), F32), pltpu.VMEM((FOX_T, 1), F32)],
                  compiler_params=_params("parallel", "parallel", "arbitrary"))(q, k, v, cq, ck, o, lse, do)


def _fox_bwd_kv(q, k, v, cq, ck, lse, do, dsum, *, name):
    H, S, dh = q.shape
    nt = S // FOX_T
    tn = (((0,), (0,)), ((), ()))

    def body(q_ref, k_ref, v_ref, cq_ref, ck_ref, l_ref, do_ref, ds_ref, dk_ref, dv_ref, dck_ref, dk_s, dv_s, dc_s):
        j, i = pl.program_id(1), pl.program_id(2)

        @pl.when(i == 0)
        def _():
            dk_s[...] = jnp.zeros((FOX_T, dh), F32)
            dv_s[...] = jnp.zeros((FOX_T, dh), F32)
            dc_s[...] = jnp.zeros((1, FOX_T), F32)

        @pl.when(i >= j)
        def _():
            s = _fox_scores(q_ref[0], k_ref[0], cq_ref[0], ck_ref[0], i, j)
            p = jnp.exp(s - l_ref[0])
            dob = do_ref[0].astype(_CD)
            dp = lax.dot_general(dob, v_ref[0], (((1,), (1,)), ((), ())), preferred_element_type=F32)
            ds = p * (dp - ds_ref[0])
            dc_s[...] += jnp.sum(ds, axis=0, keepdims=True)
            dk_s[...] += lax.dot_general((ds * ATTN_SCALE).astype(_CD), q_ref[0], tn, preferred_element_type=F32)
            dv_s[...] += lax.dot_general(p.astype(_CD), dob, tn, preferred_element_type=F32)

        @pl.when(i == nt - 1)
        def _():
            dk_ref[0] = dk_s[...].astype(dk_ref.dtype)
            dv_ref[0] = dv_s[...].astype(dv_ref.dtype)
            dck_ref[0] = dc_s[...]

    ks = pl.BlockSpec((1, FOX_T, dh), lambda h, j, i: (h, j, 0))
    qs = pl.BlockSpec((1, FOX_T, dh), lambda h, j, i: (h, jnp.maximum(i, j), 0))
    cqs = pl.BlockSpec((1, FOX_T, 1), lambda h, j, i: (h, jnp.maximum(i, j), 0))
    cks = pl.BlockSpec((1, 1, FOX_T), lambda h, j, i: (h, 0, j))
    shp = jax.ShapeDtypeStruct((H, S, dh), _CD)
    return _pcall(body, name=name, grid=(H, nt, nt), in_specs=[qs, ks, ks, cqs, cks, cqs, qs, cqs],
                  out_specs=[ks, ks, cks], out_shape=[shp, shp, jax.ShapeDtypeStruct((H, 1, S), F32)],
                  scratch_shapes=[pltpu.VMEM((FOX_T, dh), F32), pltpu.VMEM((FOX_T, dh), F32), pltpu.VMEM((1, FOX_T), F32)],
                  compiler_params=_params("parallel", "parallel", "arbitrary"))(q, k, v, cq, ck, lse, do, dsum)


def _dedilate(t, d):
    if d == 1:
        return t
    S, C = t.shape
    return t.reshape(S // d, d, C).transpose(1, 0, 2).reshape(S, C)


def _redilate(t, d):
    if d == 1:
        return t
    S, C = t.shape
    return t.reshape(d, S // d, C).transpose(1, 0, 2).reshape(S, C)


def _to_heads(t):
    S, W = t.shape
    return t.reshape(S, W // HEAD_DIM, HEAD_DIM).transpose(1, 0, 2)


def _from_heads(t):
    H, S, dh = t.shape
    return t.transpose(1, 0, 2).reshape(S, H * dh)


def _layer_step(x, tgt, w, p):
    S = x.shape[0]
    h = _rms_fwd(x, p["norm_mix_g"], name="rms_mix")
    qkv = _mm(h, w["qkv"], name="proj_qkv", out_dtype=_CD, tn=768)
    zf = _mm(h, w["f"], name="proj_f")
    gl = _mm(h, w["g"], name="proj_gate")

    dil_q, dil_k, dil_v = [], [], []
    dil_o, dil_l = [], []
    for g, (_, d) in enumerate(DIL_PAIRS):
        qg = _dedilate(qkv[:, g * DIL_OUT:(g + 1) * DIL_OUT], d)
        kg = _dedilate(qkv[:, DIL_WIDTH + g * DIL_OUT:DIL_WIDTH + (g + 1) * DIL_OUT], d)
        vg = _dedilate(qkv[:, 2 * DIL_WIDTH + g * DIL_OUT:2 * DIL_WIDTH + (g + 1) * DIL_OUT], d)
        og, lg = _dil_fwd(qg, kg, vg, g, name=f"dil_fwd{g}")
        dil_q.append(qg), dil_k.append(kg), dil_v.append(vg)
        dil_o.append(_redilate(og, d)), dil_l.append(_redilate(lg, d))
    o_a = _dil_mix_fwd(dil_o, dil_l, name="dil_mix")

    base = 3 * DIL_WIDTH
    fq = _to_heads(qkv[:, base:base + FOX_WIDTH])
    fk = _to_heads(qkv[:, base + FOX_WIDTH:base + 2 * FOX_WIDTH])
    fv = _to_heads(qkv[:, base + 2 * FOX_WIDTH:base + 3 * FOX_WIDTH])
    c = _fox_cumsum(zf, p["b_fgt"], name="fox_cumsum")
    ct = c[:, :N_FOX_HEADS].T
    cq, ck = ct[:, :, None], ct[:, None, :]
    fo, flse = _fox_fwd(fq, fk, fv, cq, ck, name="fox_fwd")
    o_b = _from_heads(fo).astype(_CD)

    y_a = _mm(o_a, w["dil_out"], name="y_a")
    y_b = _mm(o_b, w["fox_out"], name="y_b")
    merged = _gate_fwd(gl, p["b_gate"], y_a, y_b, name="gate_fwd")
    x1 = _mm(merged, w["out"], name="mix_out", add=x)

    h2 = _rms_fwd(x1, p["norm_ffn_g"], name="rms_ffn")
    gu = _mm(h2, w["ffn_in"], name="ffn_in")
    act = _swiglu_fwd(gu, name="swiglu")
    x2 = _mm(act, w["ffn_down"], name="ffn_down", add=x1)

    loss, dx2, dg_final = _loss_head(x2, p["norm_final_g"], tgt, name="loss_head")

    dact = _mm(dx2, w["ffn_down"], name="d_act", tb=True, tn=1408)
    gw_ffn_down = _mm(act, dx2, name="gw_ffn_down", ta=True, out_dtype=_CD, tm=1408, tk=1024)
    dgu = _swiglu_bwd(dact, gu, name="swiglu_bwd")
    dh2 = _mm(dgu, w["ffn_in"], name="d_h2", tb=True, tk=1408)
    gw_ffn_in = _mm(h2, dgu, name="gw_ffn_in", ta=True, out_dtype=_CD, tk=1024)
    dx1, dg_ffn = _rms_bwd(x1, p["norm_ffn_g"], dh2, dx2, name="rms_ffn_bwd")

    dmerged = _mm(dx1, w["out"], name="d_merged", tb=True)
    gw_out = _mm(merged, dx1, name="gw_out", ta=True, out_dtype=_CD, tk=1024)
    dy_a, dy_b, dgl, db_gate = _gate_bwd(dmerged, gl, p["b_gate"], y_a, y_b, name="gate_bwd")
    do_a = _mm(dy_a, w["dil_out"], name="d_o_a", tb=True)
    gw_dil_out = _mm(o_a, dy_a, name="gw_dil_out", ta=True, out_dtype=_CD, tk=1024)
    do_b = _mm(dy_b, w["fox_out"], name="d_o_b", tb=True)
    gw_fox_out = _mm(o_b, dy_b, name="gw_fox_out", ta=True, out_dtype=_CD, tk=1024)

    fdo = _to_heads(do_b)
    fdq, dcq, dsum = _fox_bwd_q(fq, fk, fv, cq, ck, fo, flse, fdo, name="fox_bwd_q")
    fdk, fdv, dck = _fox_bwd_kv(fq, fk, fv, cq, ck, flse, fdo, dsum, name="fox_bwd_kv")
    dc = (dcq[:, :, 0] - dck[:, 0, :]).T
    dc = jnp.pad(dc, ((0, 0), (0, F_PAD - N_FOX_HEADS)))
    dzf, db_fgt = _fox_cumsum_bwd(dc, zf, p["b_fgt"], name="fox_cumsum_bwd")

    douts = _dil_mix_bwd(do_a, dil_o, dil_l, name="dil_mix_bwd")
    dqs, dks, dvs = [], [], []
    for g, (_, d) in enumerate(DIL_PAIRS):
        dq, dk, dv = _dil_bwd(dil_q[g], dil_k[g], dil_v[g], _dedilate(dil_o[g], d), _dedilate(dil_l[g], d),
                              _dedilate(douts[g], d), _dedilate(douts[3 + g], d), g, name=f"dil_bwd{g}")
        dqs.append(_redilate(dq, d)), dks.append(_redilate(dk, d)), dvs.append(_redilate(dv, d))
    dqkv = jnp.concatenate(dqs + dks + dvs + [_from_heads(fdq), _from_heads(fdk), _from_heads(fdv)], axis=1)

    dh = _mm(dqkv, w["qkv"], name="d_h_qkv", tb=True, tk=1920)
    dh = _mm(dgl, w["g"], name="d_h_gate", tb=True, add=dh)
    dh = _mm(dzf, w["f"], name="d_h_f", tb=True, add=dh)
    gw_qkv = _mm(h, dqkv, name="gw_qkv", ta=True, out_dtype=_CD, tn=768, tk=1024)
    gw_g = _mm(h, dgl, name="gw_gate", ta=True, out_dtype=_CD, tk=1024)
    gw_f = _mm(h, dzf, name="gw_f", ta=True, out_dtype=_CD, tk=1024)
    dx, dg_mix = _rms_bwd(x, p["norm_mix_g"], dh, dx1, name="rms_mix_bwd")

    gw = dict(qkv=gw_qkv, f=gw_f, g=gw_g, dil_out=gw_dil_out, fox_out=gw_fox_out, out=gw_out, ffn_in=gw_ffn_in,
              ffn_down=gw_ffn_down)
    small = dict(norm_mix_g=dg_mix, b_fgt=db_fgt, b_gate=db_gate, norm_ffn_g=dg_ffn, norm_final_g=dg_final)
    return loss, dx, gw, small


def _position():
    return lax.axis_index("x"), lax.axis_index("y"), lax.axis_index("c")


def _other_chips(x, y):
    return [(1 - x, y), (x, 1 - y), (1 - x, 1 - y)]


def _gather_weights(shards):
    n = len(shards)

    def body(*refs):
        ins, outs = refs[:n], refs[n:2 * n]
        send_sems, recv_sems = refs[2 * n:]
        x, y, c = _position()
        mine = 2 * x + y
        chips = _other_chips(x, y)
        started = []
        for w in range(n):
            half = ins[w].shape[0] // 2
            rows = pl.ds(c * half, half)
            for r, (cx, cy) in enumerate(chips):
                cp = pltpu.make_async_remote_copy(
                    src_ref=ins[w].at[rows, :], dst_ref=outs[w].at[mine, rows, :], send_sem=send_sems.at[w, r],
                    recv_sem=recv_sems.at[w, r], device_id=(cx, cy, c), device_id_type=MESH)
                cp.start()
                started.append(cp)

        def landed(w, r, rows, peer):
            cx, cy = chips[r % 3]
            blk = outs[w].at[2 * cx + cy, rows, :]
            return pltpu.make_async_remote_copy(src_ref=blk, dst_ref=blk, send_sem=send_sems.at[w, r],
                                                recv_sem=recv_sems.at[w, r], device_id=peer, device_id_type=MESH)

        for w in range(n):
            half = ins[w].shape[0] // 2
            rows = pl.ds(c * half, half)
            for r, (cx, cy) in enumerate(chips):
                landed(w, r, rows, (cx, cy, c)).wait_recv()
                fwd = landed(w, 3 + r, rows, (x, y, 1 - c))
                fwd.start()
                started.append(fwd)
        for w in range(n):
            half = ins[w].shape[0] // 2
            rows = pl.ds((1 - c) * half, half)
            for r in range(3):
                landed(w, 3 + r, rows, (x, y, 1 - c)).wait_recv()
        for cp in started:
            cp.wait_send()

    return _pcall(
        body, name="gather_weights", in_specs=[HBM_SPEC] * n, out_specs=[HBM_SPEC] * n,
        out_shape=[jax.ShapeDtypeStruct((4,) + s.shape, s.dtype) for s in shards],
        scratch_shapes=[pltpu.SemaphoreType.DMA((n, 6)), pltpu.SemaphoreType.DMA((n, 6))],
    )(*shards)


def _swap_halves(grads):
    n = len(grads)

    def body(*refs):
        ins, outs = refs[:n], refs[n:2 * n]
        send_sems, recv_sems = refs[2 * n:]
        x, y, c = _position()
        copies = []
        for w in range(n):
            half = ins[w].shape[1] // 2
            cp = pltpu.make_async_remote_copy(
                src_ref=ins[w].at[:, pl.ds((1 - c) * half, half), :], dst_ref=outs[w], send_sem=send_sems.at[w],
                recv_sem=recv_sems.at[w], device_id=(x, y, 1 - c), device_id_type=MESH)
            cp.start()
            copies.append(cp)
        for cp in copies:
            cp.wait()

    return _pcall(
        body, name="swap_halves", in_specs=[HBM_SPEC] * n, out_specs=[HBM_SPEC] * n,
        out_shape=[jax.ShapeDtypeStruct((4, g.shape[1] // 2, g.shape[2]), g.dtype) for g in grads],
        scratch_shapes=[pltpu.SemaphoreType.DMA((n,)), pltpu.SemaphoreType.DMA((n,))],
    )(*grads)


def _scatter_to_owners(parts):
    n = len(parts)

    def body(*refs):
        ins, outs = refs[:n], refs[n:2 * n]
        send_sems, recv_sems = refs[2 * n:]
        x, y, c = _position()
        copies = []
        for w in range(n):
            for r, (cx, cy) in enumerate(_other_chips(x, y)):
                cp = pltpu.make_async_remote_copy(
                    src_ref=ins[w].at[2 * cx + cy], dst_ref=outs[w].at[r], send_sem=send_sems.at[w, r],
                    recv_sem=recv_sems.at[w, r], device_id=(cx, cy, c), device_id_type=MESH)
                cp.start()
                copies.append(cp)
        for cp in copies:
            cp.wait()

    return _pcall(
        body, name="scatter_to_owners", in_specs=[HBM_SPEC] * n, out_specs=[HBM_SPEC] * n,
        out_shape=[jax.ShapeDtypeStruct((3,) + p.shape[1:], p.dtype) for p in parts],
        scratch_shapes=[pltpu.SemaphoreType.DMA((n, 3)), pltpu.SemaphoreType.DMA((n, 3))],
    )(*parts)


def _share_halves(halves):
    n = len(halves)

    def body(*refs):
        ins, outs = refs[:n], refs[n:2 * n]
        send_sems, recv_sems = refs[2 * n:]
        x, y, c = _position()
        copies = []
        for w in range(n):
            cp = pltpu.make_async_remote_copy(src_ref=ins[w], dst_ref=outs[w], send_sem=send_sems.at[w],
                                              recv_sem=recv_sems.at[w], device_id=(x, y, 1 - c), device_id_type=MESH)
            cp.start()
            copies.append(cp)
        for cp in copies:
            cp.wait()

    return _pcall(
        body, name="share_halves", in_specs=[HBM_SPEC] * n, out_specs=[HBM_SPEC] * n,
        out_shape=[jax.ShapeDtypeStruct(h.shape, h.dtype) for h in halves],
        scratch_shapes=[pltpu.SemaphoreType.DMA((n,)), pltpu.SemaphoreType.DMA((n,))],
    )(*halves)


def _sum_small(part):
    rows, width = part.shape

    def body(x_ref, out_ref, all_ref, send_sems, recv_sems):
        x, y, c = _position()
        me, sibling = (x, y, c), (x, y, 1 - c)
        chips = _other_chips(x, y)

        def block(px, py, pc):
            return all_ref.at[pl.ds((4 * px + 2 * py + pc) * rows, rows), :]

        def copy(k, blk, to, src=None):
            return pltpu.make_async_remote_copy(
                src_ref=block(*blk) if src is None else src, dst_ref=block(*blk), send_sem=send_sems.at[k],
                recv_sem=recv_sems.at[k], device_id=to, device_id_type=MESH)

        all_ref[pl.ds((4 * x + 2 * y + c) * rows, rows), :] = x_ref[...]
        first = [copy(0, me, sibling, src=x_ref)]
        first += [copy(1 + j, me, (*chip, c), src=x_ref) for j, chip in enumerate(chips)]
        for cp in first:
            cp.start()
        passed = [copy(4 + j, (*chip, c), sibling) for j, chip in enumerate(chips)]
        for j, chip in enumerate(chips):
            copy(1 + j, (*chip, c), me).wait_recv()
            passed[j].start()
        copy(0, sibling, me).wait_recv()
        for j, chip in enumerate(chips):
            copy(4 + j, (*chip, 1 - c), me).wait_recv()
        for cp in first + passed:
            cp.wait_send()
        total = all_ref[0:rows, :]
        for d in range(1, 8):
            total = total + all_ref[d * rows:(d + 1) * rows, :]
        out_ref[...] = total

    vm = pl.BlockSpec(memory_space=pltpu.VMEM)
    return _pcall(
        body, name="sum_small", in_specs=[vm], out_specs=vm, out_shape=jax.ShapeDtypeStruct((rows, width), F32),
        scratch_shapes=[pltpu.VMEM((8 * rows, width), F32), pltpu.SemaphoreType.DMA((7,)), pltpu.SemaphoreType.DMA((7,))],
    )(part)


def _row_tile(R, C, itemsize=4, budget=1 << 20):
    for t in (512, 256, 128, 64, 32, 16, 8):
        if R % t == 0 and t * C * itemsize <= budget:
            return t
    return R


def _add_halves(g, recv, c, *, name):
    _, R, C = g.shape
    half = R // 2
    t = _row_tile(half, C)
    nb = half // t

    def body(c_ref, g_ref, r_ref, o_ref):
        o_ref[...] = (g_ref[...].astype(F32) + r_ref[...].astype(F32)).astype(o_ref.dtype)

    grid_spec = pltpu.PrefetchScalarGridSpec(
        num_scalar_prefetch=1, grid=(4, nb),
        in_specs=[pl.BlockSpec((1, t, C), lambda k, i, cr: (k, cr[0] * nb + i, 0)),
                  pl.BlockSpec((1, t, C), lambda k, i, cr: (k, i, 0))],
        out_specs=pl.BlockSpec((1, t, C), lambda k, i, cr: (k, i, 0)))
    return _pcall(body, name=name, grid_spec=grid_spec, out_shape=jax.ShapeDtypeStruct((4, half, C), g.dtype),
                  compiler_params=_params("parallel", "parallel"))(c, g, recv)


def _add_owners(mine, recv, *, name):
    half, C = mine.shape
    t = _row_tile(half, C)

    def body(m_ref, r_ref, o_ref):
        o_ref[...] = ((m_ref[...].astype(F32) + r_ref[0].astype(F32)) + r_ref[1].astype(F32)) + r_ref[2].astype(F32)

    return _pcall(body, name=name, grid=(half // t,),
                  in_specs=[pl.BlockSpec((t, C), lambda i: (i, 0)), pl.BlockSpec((3, t, C), lambda i: (0, i, 0))],
                  out_specs=pl.BlockSpec((t, C), lambda i: (i, 0)), out_shape=jax.ShapeDtypeStruct((half, C), F32),
                  compiler_params=_params("parallel"))(mine, recv)


def _adamw(w, g, m, v, *, name):
    R, C = w.shape
    t = _row_tile(R, C)
    c1 = 1.0 - ADAM_B1 ** ADAM_STEP
    c2 = 1.0 - ADAM_B2 ** ADAM_STEP

    def body(w_ref, g_ref, m_ref, v_ref, d_ref, nm_ref, nv_ref):
        gv = g_ref[...]
        mn = ADAM_B1 * m_ref[...] + (1.0 - ADAM_B1) * gv
        vn = ADAM_B2 * v_ref[...] + (1.0 - ADAM_B2) * (gv * gv)
        d_ref[...] = -ADAM_LR * ((mn / c1) / (jnp.sqrt(vn / c2) + ADAM_EPS) + ADAM_WD * w_ref[...])
        nm_ref[...] = mn
        nv_ref[...] = vn

    blk = pl.BlockSpec((t, C), lambda i: (i, 0))
    shp = jax.ShapeDtypeStruct((R, C), F32)
    return _pcall(body, name=name, grid=(R // t,), in_specs=[blk] * 4, out_specs=[blk] * 3, out_shape=[shp] * 3,
                  compiler_params=_params("parallel"))(w, g, m, v)


BIG = ("w_in", "w_dil_out", "w_fox_out", "w_out", "w_ffn_in", "w_ffn_down")
SMALL = ("norm_mix_g", "b_fgt", "b_gate", "norm_ffn_g", "norm_final_g")
ORDER = ("norm_mix_g", "w_in", "b_fgt", "b_gate", "w_dil_out", "w_fox_out", "w_out", "norm_ffn_g", "w_ffn_in",
         "w_ffn_down", "norm_final_g")
SMALL_ROWS = {"norm_mix_g": (0, 1), "b_gate": (1, 3), "norm_ffn_g": (3, 4), "norm_final_g": (4, 5), "b_fgt": (5, 6)}


def _columns_to_blocks(full, ncol):
    K = full.shape[0]
    return full.reshape(K, 4, ncol).transpose(1, 0, 2)


def _blocks_to_columns(blocks):
    n, K, ncol = blocks.shape
    return blocks.transpose(1, 0, 2).reshape(K, n * ncol)


def kernel(x, norm_mix_g, w_in, b_fgt, b_gate, w_dil_out, w_fox_out, w_out, norm_ffn_g, w_ffn_in, w_ffn_down, norm_final_g, loss_target, m_norm_mix_g, m_w_in, m_b_fgt, m_b_gate, m_w_dil_out, m_w_fox_out, m_w_out, m_norm_ffn_g, m_w_ffn_in, m_w_ffn_down, m_norm_final_g, v_norm_mix_g, v_w_in, v_b_fgt, v_b_gate, v_w_dil_out, v_w_fox_out, v_w_out, v_norm_ffn_g, v_w_ffn_in, v_w_ffn_down, v_norm_final_g):
    weights = dict(norm_mix_g=norm_mix_g, w_in=w_in, b_fgt=b_fgt, b_gate=b_gate, w_dil_out=w_dil_out,
                   w_fox_out=w_fox_out, w_out=w_out, norm_ffn_g=norm_ffn_g, w_ffn_in=w_ffn_in, w_ffn_down=w_ffn_down,
                   norm_final_g=norm_final_g)
    m_in = dict(norm_mix_g=m_norm_mix_g, w_in=m_w_in, b_fgt=m_b_fgt, b_gate=m_b_gate, w_dil_out=m_w_dil_out,
                w_fox_out=m_w_fox_out, w_out=m_w_out, norm_ffn_g=m_norm_ffn_g, w_ffn_in=m_w_ffn_in,
                w_ffn_down=m_w_ffn_down, norm_final_g=m_norm_final_g)
    v_in = dict(norm_mix_g=v_norm_mix_g, w_in=v_w_in, b_fgt=v_b_fgt, b_gate=v_b_gate, w_dil_out=v_w_dil_out,
                w_fox_out=v_w_fox_out, w_out=v_w_out, norm_ffn_g=v_norm_ffn_g, w_ffn_in=v_w_ffn_in,
                w_ffn_down=v_w_ffn_down, norm_final_g=v_norm_final_g)
    c = lax.axis_index("c")
    chip = 2 * lax.axis_index("x") + lax.axis_index("y")

    shards = [weights[n][0].astype(_CD) for n in BIG]
    gathered = _gather_weights(shards)
    g_in, g_dil, g_fox, g_out, g_ffn_in, g_ffn_down = [
        lax.dynamic_update_index_in_dim(g, s, chip, 0) for g, s in zip(gathered, shards)]
    full_in = _blocks_to_columns(g_in)
    o3 = QKV_COLS
    o4 = o3 + N_FOX_HEADS
    w = dict(
        qkv=full_in[:, :o3],
        f=jnp.pad(full_in[:, o3:o4], ((0, 0), (0, F_PAD - N_FOX_HEADS))),
        g=full_in[:, o4:],
        dil_out=_blocks_to_columns(g_dil), fox_out=_blocks_to_columns(g_fox),
        out=g_out.reshape(D_MODEL, D_MODEL), ffn_in=_blocks_to_columns(g_ffn_in),
        ffn_down=g_ffn_down.reshape(D_FF, D_MODEL))
    p = dict(norm_mix_g=norm_mix_g, b_fgt=jnp.pad(b_fgt, ((0, 0), (0, F_PAD - N_FOX_HEADS))), b_gate=b_gate,
             norm_ffn_g=norm_ffn_g, norm_final_g=norm_final_g.reshape(1, D_MODEL))

    loss_part, grad_x, gw, small = _layer_step(x[0], loss_target[0], w, p)

    gw_in = jnp.concatenate([gw["qkv"], gw["f"][:, :N_FOX_HEADS], gw["g"]], axis=1)
    blocks = [
        _columns_to_blocks(gw_in, w_in.shape[2]),
        _columns_to_blocks(gw["dil_out"], w_dil_out.shape[2]),
        _columns_to_blocks(gw["fox_out"], w_fox_out.shape[2]),
        gw["out"].reshape(4, w_out.shape[1], D_MODEL),
        _columns_to_blocks(gw["ffn_in"], w_ffn_in.shape[2]),
        gw["ffn_down"].reshape(4, w_ffn_down.shape[1], D_MODEL),
    ]
    from_sibling = _swap_halves(blocks)
    c_arr = jnp.reshape(c, (1,)).astype(jnp.int32)
    chip_sums = [_add_halves(b, r, c_arr, name=f"add_halves_{n}") for b, r, n in zip(blocks, from_sibling, BIG)]
    from_chips = _scatter_to_owners(chip_sums)
    halves = [_add_owners(lax.dynamic_index_in_dim(s, chip, 0, keepdims=False), r, name=f"add_owners_{n}")
              for s, r, n in zip(chip_sums, from_chips, BIG)]
    grads = {}
    for n, own, other in zip(BIG, halves, _share_halves(halves)):
        pair = jnp.stack([own, other])
        grads[n] = jnp.where(c == 0, pair, pair[::-1]).reshape(2 * own.shape[0], own.shape[1])

    packed = jnp.concatenate([
        small["norm_mix_g"], small["b_gate"].reshape(2, D_MODEL), small["norm_ffn_g"], small["norm_final_g"],
        jnp.pad(small["b_fgt"], ((0, 0), (0, D_MODEL - F_PAD))), jnp.zeros((2, D_MODEL), F32)], axis=0)
    summed = _sum_small(packed)
    for n in SMALL:
        lo, hi = SMALL_ROWS[n]
        grads[n] = summed[lo:hi].reshape(1, -1)[:, :weights[n].size]

    loss = lax.psum(loss_part[0, 0], ("x", "y", "c"))

    out_g, out_d, out_m, out_v = {}, {}, {}, {}
    for n in ORDER:
        shape = weights[n].shape
        two_d = shape[1:] if len(shape) == 3 else (1, weights[n].size)
        g2 = grads[n].reshape(two_d)
        d2, m2, v2 = _adamw(weights[n].reshape(two_d), g2, m_in[n].reshape(two_d), v_in[n].reshape(two_d),
                            name=f"adamw_{n}")
        out_g[n], out_d[n], out_m[n], out_v[n] = (g2.reshape(shape), d2.reshape(shape), m2.reshape(shape),
                                                  v2.reshape(shape))
    return (loss, grad_x[None], *[out_g[n] for n in ORDER], *[out_d[n] for n in ORDER],
            *[out_m[n] for n in ORDER], *[out_v[n] for n in ORDER])
```

```python
import numpy as np
import jax
import jax.numpy as jnp
from jax import lax
from jax.experimental import pallas as pl
from jax.experimental.pallas import tpu as pltpu

F32 = jnp.float32
_CD = jnp.bfloat16

D_MODEL = 1024
HEAD_DIM = 64
DIL_PAIRS = ((128, 1), (512, 4), (2048, 16))
N_DIL_GROUPS = 3
DIL_HEADS = 4
DIL_W = 128
DIL_OUT = DIL_HEADS * HEAD_DIM
DIL_WIDTH = N_DIL_GROUPS * DIL_OUT
N_FOX_HEADS = 8
FOX_WIDTH = N_FOX_HEADS * HEAD_DIM
D_FF = 2816
QKV_COLS = 3 * DIL_WIDTH + 3 * FOX_WIDTH
F_PAD = 128
RMS_EPS = 1e-6
NEG_INF = -1e30
ATTN_SCALE = HEAD_DIM ** -0.5
ADAM_LR, ADAM_B1, ADAM_B2, ADAM_EPS, ADAM_WD, ADAM_STEP = 0.001, 0.9, 0.999, 1e-08, 0.01, 10

VMEM_LIMIT = 48 * 1024 * 1024
MESH = pl.DeviceIdType.MESH
HBM_SPEC = pl.BlockSpec(memory_space=pltpu.HBM)


def _pcall(body, **kw):
    return pl.pallas_call(body, **kw)


def _params(*sem):
    return pltpu.CompilerParams(dimension_semantics=sem, vmem_limit_bytes=VMEM_LIMIT)


def _pick(dim, pref):
    t = (min(pref, dim) // 128) * 128
    while t >= 128:
        if dim % t == 0:
            return t
        t -= 128
    return dim


def _mm(a, b, *, name, ta=False, tb=False, out_dtype=F32, add=None, tm=512, tn=512, tk=2048):
    if ta:
        K, M = a.shape
    else:
        M, K = a.shape
    if tb:
        N, K2 = b.shape
    else:
        K2, N = b.shape
    assert K == K2, (a.shape, b.shape)
    tm, tn, tk = _pick(M, tm), _pick(N, tn), _pick(K, tk)
    nk = K // tk
    dn = (((0 if ta else 1,), (1 if tb else 0,)), ((), ()))
    has_add = add is not None

    def body(*refs):
        a_ref, b_ref = refs[0], refs[1]
        add_ref = refs[2] if has_add else None
        o_ref = refs[3] if has_add else refs[2]
        p = lax.dot_general(a_ref[...].astype(_CD), b_ref[...].astype(_CD), dn, preferred_element_type=F32)

        def finish(r):
            if has_add:
                r = r + add_ref[...]
            o_ref[...] = r.astype(out_dtype)

        if nk == 1:
            finish(p)
        else:
            acc_ref = refs[-1]
            k = pl.program_id(2)

            @pl.when(k == 0)
            def _():
                acc_ref[...] = p

            @pl.when(k > 0)
            def _():
                acc_ref[...] += p

            @pl.when(k == nk - 1)
            def _():
                finish(acc_ref[...])

    a_spec = pl.BlockSpec((tk, tm), lambda i, j, k: (k, i)) if ta else pl.BlockSpec((tm, tk), lambda i, j, k: (i, k))
    b_spec = pl.BlockSpec((tn, tk), lambda i, j, k: (j, k)) if tb else pl.BlockSpec((tk, tn), lambda i, j, k: (k, j))
    o_spec = pl.BlockSpec((tm, tn), lambda i, j, k: (i, j))
    in_specs = [a_spec, b_spec] + ([o_spec] if has_add else [])
    args = (a, b) + ((add,) if has_add else ())
    return _pcall(
        body, name=name, grid=(M // tm, N // tn, nk), in_specs=in_specs, out_specs=o_spec,
        out_shape=jax.ShapeDtypeStruct((M, N), out_dtype),
        scratch_shapes=[pltpu.VMEM((tm, tn), F32)] if nk > 1 else [],
        compiler_params=_params("parallel", "parallel", "arbitrary"),
    )(*args)


def _rms_fwd(x, g, *, name, tm=512):
    S, D = x.shape

    def body(x_ref, g_ref, h_ref):
        xv = x_ref[...]
        r = lax.rsqrt(jnp.mean(xv * xv, axis=-1, keepdims=True) + RMS_EPS)
        h_ref[...] = ((xv * r) * g_ref[...]).astype(h_ref.dtype)

    row = pl.BlockSpec((tm, D), lambda i: (i, 0))
    return _pcall(body, name=name, grid=(S // tm,), in_specs=[row, pl.BlockSpec((1, D), lambda i: (0, 0))],
                  out_specs=row, out_shape=jax.ShapeDtypeStruct((S, D), _CD), compiler_params=_params("parallel"))(x, g)


def _rms_bwd(x, g, dh, dres, *, name, tm=512):
    S, D = x.shape

    def body(x_ref, g_ref, dh_ref, dres_ref, dx_ref, dg_ref):
        xv = x_ref[...]
        r = lax.rsqrt(jnp.mean(xv * xv, axis=-1, keepdims=True) + RMS_EPS)
        xh = xv * r
        dhv = dh_ref[...]
        dxh = dhv * g_ref[...]
        dx_ref[...] = dres_ref[...] + r * (dxh - xh * jnp.mean(dxh * xh, axis=-1, keepdims=True))
        part = jnp.sum(dhv * xh, axis=0, keepdims=True)

        @pl.when(pl.program_id(0) == 0)
        def _():
            dg_ref[...] = part

        @pl.when(pl.program_id(0) > 0)
        def _():
            dg_ref[...] += part

    row = pl.BlockSpec((tm, D), lambda i: (i, 0))
    vec = pl.BlockSpec((1, D), lambda i: (0, 0))
    return _pcall(body, name=name, grid=(S // tm,), in_specs=[row, vec, row, row], out_specs=[row, vec],
                  out_shape=[jax.ShapeDtypeStruct((S, D), F32), jax.ShapeDtypeStruct((1, D), F32)],
                  compiler_params=_params("arbitrary"))(x, g, dh, dres)


def _loss_head(x, g, tgt, *, name, tm=512):
    S, D = x.shape

    def body(x_ref, g_ref, t_ref, loss_ref, dx_ref, dg_ref):
        xv = x_ref[...]
        gv = g_ref[...]
        r = lax.rsqrt(jnp.mean(xv * xv, axis=-1, keepdims=True) + RMS_EPS)
        xh = xv * r
        err = xh * gv - t_ref[...]
        lpart = 0.5 * jnp.sum(jnp.mean(err * err, axis=-1, keepdims=True), axis=0, keepdims=True)
        dy = err * (1.0 / D)
        dxh = dy * gv
        dx_ref[...] = r * (dxh - xh * jnp.mean(dxh * xh, axis=-1, keepdims=True))
        gpart = jnp.sum(dy * xh, axis=0, keepdims=True)

        @pl.when(pl.program_id(0) == 0)
        def _():
            loss_ref[...] = lpart
            dg_ref[...] = gpart

        @pl.when(pl.program_id(0) > 0)
        def _():
            loss_ref[...] += lpart
            dg_ref[...] += gpart

    row = pl.BlockSpec((tm, D), lambda i: (i, 0))
    vec = pl.BlockSpec((1, D), lambda i: (0, 0))
    one = pl.BlockSpec((1, 1), lambda i: (0, 0))
    return _pcall(body, name=name, grid=(S // tm,), in_specs=[row, vec, row], out_specs=[one, row, vec],
                  out_shape=[jax.ShapeDtypeStruct((1, 1), F32), jax.ShapeDtypeStruct((S, D), F32),
                             jax.ShapeDtypeStruct((1, D), F32)],
                  compiler_params=_params("arbitrary"))(x, g, tgt)


def _sigmoid(z):
    return 1.0 / (1.0 + jnp.exp(-z))


def _gate_fwd(gl, bg, ya, yb, *, name, tm=512):
    S, D = ya.shape

    def body(za_ref, zb_ref, ba_ref, bb_ref, ya_ref, yb_ref, o_ref):
        ga = _sigmoid(za_ref[...] + ba_ref[...])
        gb = _sigmoid(zb_ref[...] + bb_ref[...])
        o_ref[...] = (ga * ya_ref[...] + gb * yb_ref[...]).astype(o_ref.dtype)

    lo = pl.BlockSpec((tm, D), lambda i: (i, 0))
    hi = pl.BlockSpec((tm, D), lambda i: (i, 1))
    vlo = pl.BlockSpec((1, D), lambda i: (0, 0))
    vhi = pl.BlockSpec((1, D), lambda i: (0, 1))
    return _pcall(body, name=name, grid=(S // tm,), in_specs=[lo, hi, vlo, vhi, lo, lo], out_specs=lo,
                  out_shape=jax.ShapeDtypeStruct((S, D), _CD), compiler_params=_params("parallel"))(gl, gl, bg, bg, ya, yb)


def _gate_bwd(dm, gl, bg, ya, yb, *, name, tm=256):
    S, D = ya.shape

    def body(dm_ref, za_ref, zb_ref, ba_ref, bb_ref, ya_ref, yb_ref, dya_ref, dyb_ref, dgl_ref, dbg_ref):
        dmv = dm_ref[...]
        ga = _sigmoid(za_ref[...] + ba_ref[...])
        gb = _sigmoid(zb_ref[...] + bb_ref[...])
        dya_ref[...] = (dmv * ga).astype(dya_ref.dtype)
        dyb_ref[...] = (dmv * gb).astype(dyb_ref.dtype)
        dza = dmv * ya_ref[...] * ga * (1.0 - ga)
        dzb = dmv * yb_ref[...] * gb * (1.0 - gb)
        dgl_ref[:, :D] = dza.astype(dgl_ref.dtype)
        dgl_ref[:, D:] = dzb.astype(dgl_ref.dtype)
        pa = jnp.sum(dza, axis=0, keepdims=True)
        pb = jnp.sum(dzb, axis=0, keepdims=True)

        @pl.when(pl.program_id(0) == 0)
        def _():
            dbg_ref[:, :D] = pa
            dbg_ref[:, D:] = pb

        @pl.when(pl.program_id(0) > 0)
        def _():
            dbg_ref[:, :D] += pa
            dbg_ref[:, D:] += pb

    lo = pl.BlockSpec((tm, D), lambda i: (i, 0))
    hi = pl.BlockSpec((tm, D), lambda i: (i, 1))
    vlo = pl.BlockSpec((1, D), lambda i: (0, 0))
    vhi = pl.BlockSpec((1, D), lambda i: (0, 1))
    wide = pl.BlockSpec((tm, 2 * D), lambda i: (i, 0))
    vwide = pl.BlockSpec((1, 2 * D), lambda i: (0, 0))
    return _pcall(body, name=name, grid=(S // tm,), in_specs=[lo, lo, hi, vlo, vhi, lo, lo],
                  out_specs=[lo, lo, wide, vwide],
                  out_shape=[jax.ShapeDtypeStruct((S, D), _CD), jax.ShapeDtypeStruct((S, D), _CD),
                             jax.ShapeDtypeStruct((S, 2 * D), _CD), jax.ShapeDtypeStruct((1, 2 * D), F32)],
                  compiler_params=_params("arbitrary"))(dm, gl, gl, bg, bg, ya, yb)


def _swiglu_fwd(gu, *, name, tm=256):
    S, F2 = gu.shape
    F = F2 // 2

    def body(g_ref, u_ref, o_ref):
        gv = g_ref[...]
        o_ref[...] = (gv * _sigmoid(gv) * u_ref[...]).astype(o_ref.dtype)

    lo = pl.BlockSpec((tm, F), lambda i: (i, 0))
    hi = pl.BlockSpec((tm, F), lambda i: (i, 1))
    return _pcall(body, name=name, grid=(S // tm,), in_specs=[lo, hi], out_specs=lo,
                  out_shape=jax.ShapeDtypeStruct((S, F), _CD), compiler_params=_params("parallel"))(gu, gu)


def _swiglu_bwd(dact, gu, *, name, tm=256):
    S, F2 = gu.shape
    F = F2 // 2

    def body(d_ref, g_ref, u_ref, o_ref):
        dv = d_ref[...]
        gv = g_ref[...]
        sg = _sigmoid(gv)
        o_ref[:, :F] = (dv * u_ref[...] * (sg * (1.0 + gv * (1.0 - sg)))).astype(o_ref.dtype)
        o_ref[:, F:] = (dv * (gv * sg)).astype(o_ref.dtype)

    lo = pl.BlockSpec((tm, F), lambda i: (i, 0))
    hi = pl.BlockSpec((tm, F), lambda i: (i, 1))
    return _pcall(body, name=name, grid=(S // tm,), in_specs=[lo, lo, hi],
                  out_specs=pl.BlockSpec((tm, F2), lambda i: (i, 0)),
                  out_shape=jax.ShapeDtypeStruct((S, F2), _CD), compiler_params=_params("parallel"))(dact, gu, gu)


def _split3(x):
    hi = x.astype(jnp.bfloat16)
    r1 = x - hi.astype(F32)
    mid = r1.astype(jnp.bfloat16)
    lo = (r1 - mid.astype(F32)).astype(jnp.bfloat16)
    return hi, mid, lo


def _ones_dot_left(ones, x):
    return sum(jnp.dot(ones, p, preferred_element_type=F32) for p in _split3(x))


def _ones_dot_right(x, ones):
    return sum(jnp.dot(p, ones, preferred_element_type=F32) for p in _split3(x))


def _head_sum(x):
    n = x.shape[1]
    r = lax.broadcasted_iota(jnp.int32, (n, n), 0) // HEAD_DIM
    c = lax.broadcasted_iota(jnp.int32, (n, n), 1) // HEAD_DIM
    return _ones_dot_right(x, (r == c).astype(jnp.bfloat16))


def _log_sigmoid(z):
    e = jnp.exp(-jnp.abs(z))
    t = 1.0 + e
    log1p_e = jnp.where(t == 1.0, e, jnp.log(t) * (e / jnp.where(t == 1.0, 1.0, t - 1.0)))
    return jnp.minimum(z, 0.0) - log1p_e


def _fox_cumsum(zf, bf, *, name):
    S, W = zf.shape
    nb = S // 128

    def body(z_ref, b_ref, c_ref):
        tri = (lax.broadcasted_iota(jnp.int32, (128, 128), 0) >= lax.broadcasted_iota(jnp.int32, (128, 128), 1))
        tri = tri.astype(jnp.bfloat16)

        def step(i, carry):
            rows = pl.ds(pl.multiple_of(i * 128, 128), 128)
            lf = _log_sigmoid(z_ref[rows, :] + b_ref[...])
            cb = _ones_dot_left(tri, lf) + carry
            c_ref[rows, :] = cb
            return cb[127:128, :]

        lax.fori_loop(0, nb, step, jnp.zeros((1, W), F32))

    return _pcall(body, name=name, out_shape=jax.ShapeDtypeStruct((S, W), F32),
                  compiler_params=pltpu.CompilerParams(vmem_limit_bytes=VMEM_LIMIT))(zf, bf)


def _fox_cumsum_bwd(dc, zf, bf, *, name):
    S, W = zf.shape
    nb = S // 128

    def body(dc_ref, z_ref, b_ref, dz_ref, db_ref):
        tri = (lax.broadcasted_iota(jnp.int32, (128, 128), 0) <= lax.broadcasted_iota(jnp.int32, (128, 128), 1))
        tri = tri.astype(jnp.bfloat16)

        def step(k, carry):
            tail, acc = carry
            i = nb - 1 - k
            rows = pl.ds(pl.multiple_of(i * 128, 128), 128)
            dlf = _ones_dot_left(tri, dc_ref[rows, :]) + tail
            dz = dlf * _sigmoid(-(z_ref[rows, :] + b_ref[...]))
            dz_ref[rows, :] = dz
            return dlf[0:1, :], acc + jnp.sum(dz, axis=0, keepdims=True)

        _, acc = lax.fori_loop(0, nb, step, (jnp.zeros((1, W), F32), jnp.zeros((1, W), F32)))
        db_ref[...] = acc

    return _pcall(body, name=name,
                  out_shape=[jax.ShapeDtypeStruct((S, W), F32), jax.ShapeDtypeStruct((1, W), F32)],
                  compiler_params=pltpu.CompilerParams(vmem_limit_bytes=VMEM_LIMIT))(dc, zf, bf)


def _dil_slopes(group):
    h = np.arange(1, N_DIL_GROUPS * DIL_HEADS + 1, dtype=np.float32)
    s = (np.float32(2.0) ** (np.float32(-8.0) * h / np.float32(N_DIL_GROUPS * DIL_HEADS))).astype(np.float32)
    return [float(v) for v in s.reshape(N_DIL_GROUPS, DIL_HEADS)[group]]


def _dil_tiles(i, n, blocks_per_seq):
    qi = lax.broadcasted_iota(jnp.int32, (DIL_W, DIL_W), 0)
    kj = lax.broadcasted_iota(jnp.int32, (DIL_W, DIL_W), 1)
    first = ((4 * n + i) % blocks_per_seq) == 0
    valid_prev = jnp.logical_and(kj >= qi, jnp.logical_not(first))
    valid_cur = kj <= qi
    rel_prev = (qi - kj + DIL_W).astype(F32)
    rel_cur = (qi - kj).astype(F32)
    return valid_prev, valid_cur, rel_prev, rel_cur


CHUNK = 4 * DIL_W


def _dil_fwd(q, k, v, group, *, name):
    S = q.shape[0]
    dilation = DIL_PAIRS[group][1]
    bps = (S // dilation) // DIL_W
    slopes = _dil_slopes(group)
    nt = (((1,), (1,)), ((), ()))

    def body(q_ref, k_ref, v_ref, kp_ref, vp_ref, o_ref, l_ref):
        n = pl.program_id(0)
        for i in range(4):
            valid_prev, valid_cur, rel_prev, rel_cur = _dil_tiles(i, n, bps)
            rows = slice(i * DIL_W, (i + 1) * DIL_W)
            prow = slice((i - 1) * DIL_W, i * DIL_W)
            for h in range(DIL_HEADS):
                cols = slice(h * HEAD_DIM, (h + 1) * HEAD_DIM)
                qh = q_ref[rows, cols]
                kc, vc = k_ref[rows, cols], v_ref[rows, cols]
                kp = kp_ref[:, cols] if i == 0 else k_ref[prow, cols]
                vp = vp_ref[:, cols] if i == 0 else v_ref[prow, cols]
                sl = slopes[h] * dilation
                sp = lax.dot_general(qh, kp, nt, preferred_element_type=F32) * ATTN_SCALE - sl * rel_prev
                sc = lax.dot_general(qh, kc, nt, preferred_element_type=F32) * ATTN_SCALE - sl * rel_cur
                sp = jnp.where(valid_prev, sp, NEG_INF)
                sc = jnp.where(valid_cur, sc, NEG_INF)
                m = jnp.maximum(jnp.max(sp, axis=-1, keepdims=True), jnp.max(sc, axis=-1, keepdims=True))
                pp, pc = jnp.exp(sp - m), jnp.exp(sc - m)
                den = jnp.sum(pp, axis=-1, keepdims=True) + jnp.sum(pc, axis=-1, keepdims=True)
                acc = (jnp.dot(pp.astype(_CD), vp, preferred_element_type=F32)
                       + jnp.dot(pc.astype(_CD), vc, preferred_element_type=F32))
                o_ref[rows, cols] = acc / den
                l_ref[rows, cols] = jnp.broadcast_to(m + jnp.log(den), (DIL_W, HEAD_DIM))

    cur = pl.BlockSpec((CHUNK, DIL_OUT), lambda n: (n, 0))
    prev = pl.BlockSpec((DIL_W, DIL_OUT), lambda n: (jnp.maximum(4 * n - 1, 0), 0))
    return _pcall(body, name=name, grid=(S // CHUNK,), in_specs=[cur, cur, cur, prev, prev], out_specs=[cur, cur],
                  out_shape=[jax.ShapeDtypeStruct((S, DIL_OUT), F32), jax.ShapeDtypeStruct((S, DIL_OUT), F32)],
                  compiler_params=_params("parallel"))(q, k, v, k, v)


def _dil_bwd(q, k, v, o, lse, do, dlse, group, *, name):
    S = q.shape[0]
    dilation = DIL_PAIRS[group][1]
    bps = (S // dilation) // DIL_W
    slopes = _dil_slopes(group)
    nchunk = S // CHUNK
    nt = (((1,), (1,)), ((), ()))
    tn = (((0,), (0,)), ((), ()))

    def body(q_ref, k_ref, v_ref, kp_ref, vp_ref, o_ref, l_ref, do_ref, dl_ref, dq_ref, dk_ref, dv_ref,
             dk_s, dv_s):
        step = pl.program_id(0)
        n = nchunk - 1 - step

        @pl.when(step == 0)
        def _():
            dk_s[CHUNK:, :] = jnp.zeros((DIL_W, DIL_OUT), F32)
            dv_s[CHUNK:, :] = jnp.zeros((DIL_W, DIL_OUT), F32)

        dk_s[:CHUNK, :] = jnp.zeros((CHUNK, DIL_OUT), F32)
        dv_s[:CHUNK, :] = jnp.zeros((CHUNK, DIL_OUT), F32)
        for i in range(4):
            valid_prev, valid_cur, rel_prev, rel_cur = _dil_tiles(i, n, bps)
            rows = slice(i * DIL_W, (i + 1) * DIL_W)
            prow = slice((i - 1) * DIL_W, i * DIL_W)
            s_prev = slice(i * DIL_W, (i + 1) * DIL_W)
            s_cur = slice((i + 1) * DIL_W, (i + 2) * DIL_W)
            for h in range(DIL_HEADS):
                cols = slice(h * HEAD_DIM, (h + 1) * HEAD_DIM)
                qh = q_ref[rows, cols]
                kc, vc = k_ref[rows, cols], v_ref[rows, cols]
                kp = kp_ref[:, cols] if i == 0 else k_ref[prow, cols]
                vp = vp_ref[:, cols] if i == 0 else v_ref[prow, cols]
                sl = slopes[h] * dilation
                lh = l_ref[rows, h * HEAD_DIM:h * HEAD_DIM + 1]
                sp = lax.dot_general(qh, kp, nt, preferred_element_type=F32) * ATTN_SCALE - sl * rel_prev
                sc = lax.dot_general(qh, kc, nt, preferred_element_type=F32) * ATTN_SCALE - sl * rel_cur
                pp = jnp.exp(jnp.where(valid_prev, sp, NEG_INF) - lh)
                pc = jnp.exp(jnp.where(valid_cur, sc, NEG_INF) - lh)
                doh = do_ref[rows, cols]
                dsum = jnp.sum(doh * o_ref[rows, cols], axis=-1, keepdims=True)
                shift = dl_ref[rows, h * HEAD_DIM:h * HEAD_DIM + 1] - dsum
                dob = doh.astype(_CD)
                dsp = pp * (lax.dot_general(dob, vp, nt, preferred_element_type=F32) + shift)
                dsc = pc * (lax.dot_general(dob, vc, nt, preferred_element_type=F32) + shift)
                dspb = (dsp * ATTN_SCALE).astype(_CD)
                dscb = (dsc * ATTN_SCALE).astype(_CD)
                dq_ref[rows, cols] = (jnp.dot(dspb, kp, preferred_element_type=F32)
                                      + jnp.dot(dscb, kc, preferred_element_type=F32)).astype(dq_ref.dtype)
                dk_s[s_prev, cols] += lax.dot_general(dspb, qh, tn, preferred_element_type=F32)
                dk_s[s_cur, cols] += lax.dot_general(dscb, qh, tn, preferred_element_type=F32)
                dv_s[s_prev, cols] += lax.dot_general(pp.astype(_CD), dob, tn, preferred_element_type=F32)
                dv_s[s_cur, cols] += lax.dot_general(pc.astype(_CD), dob, tn, preferred_element_type=F32)
        dk_ref[...] = dk_s[DIL_W:, :].astype(dk_ref.dtype)
        dv_ref[...] = dv_s[DIL_W:, :].astype(dv_ref.dtype)
        dk_s[CHUNK:, :] = dk_s[:DIL_W, :]
        dv_s[CHUNK:, :] = dv_s[:DIL_W, :]

    cur = pl.BlockSpec((CHUNK, DIL_OUT), lambda s: (nchunk - 1 - s, 0))
    prev = pl.BlockSpec((DIL_W, DIL_OUT), lambda s: (jnp.maximum(4 * (nchunk - 1 - s) - 1, 0), 0))
    shp = jax.ShapeDtypeStruct((S, DIL_OUT), _CD)
    return _pcall(body, name=name, grid=(nchunk,), in_specs=[cur, cur, cur, prev, prev, cur, cur, cur, cur],
                  out_specs=[cur, cur, cur], out_shape=[shp, shp, shp],
                  scratch_shapes=[pltpu.VMEM((CHUNK + DIL_W, DIL_OUT), F32), pltpu.VMEM((CHUNK + DIL_W, DIL_OUT), F32)],
                  compiler_params=_params("arbitrary"))(q, k, v, k, v, o, lse, do, dlse)


def _dil_mix_fwd(os_, ls_, *, name, tm=512):
    S, W = os_[0].shape

    def body(o0, o1, o2, l0, l1, l2, out_ref):
        ls = [l0[...], l1[...], l2[...]]
        m = jnp.maximum(jnp.maximum(ls[0], ls[1]), ls[2])
        es = [jnp.exp(l - m) for l in ls]
        den = es[0] + es[1] + es[2]
        out_ref[...] = ((es[0] * o0[...] + es[1] * o1[...] + es[2] * o2[...]) / den).astype(out_ref.dtype)

    row = pl.BlockSpec((tm, W), lambda i: (i, 0))
    return _pcall(body, name=name, grid=(S // tm,), in_specs=[row] * 6, out_specs=row,
                  out_shape=jax.ShapeDtypeStruct((S, W), _CD), compiler_params=_params("parallel"))(*os_, *ls_)


def _dil_mix_bwd(doa, os_, ls_, *, name, tm=512):
    S, W = doa.shape

    def body(d_ref, o0, o1, o2, l0, l1, l2, do0, do1, do2, dl0, dl1, dl2):
        dv = d_ref[...]
        ls = [l0[...], l1[...], l2[...]]
        m = jnp.maximum(jnp.maximum(ls[0], ls[1]), ls[2])
        es = [jnp.exp(l - m) for l in ls]
        den = es[0] + es[1] + es[2]
        al = [e / den for e in es]
        da = [_head_sum(dv * o[...]) for o in (o0, o1, o2)]
        mean = al[0] * da[0] + al[1] * da[1] + al[2] * da[2]
        for a, d_, do_ref, dl_ref in zip(al, da, (do0, do1, do2), (dl0, dl1, dl2)):
            do_ref[...] = a * dv
            dl_ref[...] = a * (d_ - mean)

    row = pl.BlockSpec((tm, W), lambda i: (i, 0))
    shp = jax.ShapeDtypeStruct((S, W), F32)
    return _pcall(body, name=name, grid=(S // tm,), in_specs=[row] * 7, out_specs=[row] * 6, out_shape=[shp] * 6,
                  compiler_params=_params("parallel"))(doa, *os_, *ls_)


FOX_T = 512


PACK = 2 * HEAD_DIM
HEAD_PAIRS = N_FOX_HEADS // 2
Q_BLOCK0 = (3 * DIL_WIDTH) // PACK
K_BLOCK0 = (3 * DIL_WIDTH + FOX_WIDTH) // PACK
V_BLOCK0 = (3 * DIL_WIDTH + 2 * FOX_WIDTH) // PACK


def _pieces(x):
    hi = x.astype(jnp.bfloat16).astype(F32)
    r = x - hi
    mid = r.astype(jnp.bfloat16).astype(F32)
    lo = (r - mid).astype(jnp.bfloat16).astype(F32)
    return [hi, mid, lo]


def _extras(first, second, rows):
    lane = lax.broadcasted_iota(jnp.int32, (rows, HEAD_DIM), 1)
    out = jnp.zeros((rows, HEAD_DIM), F32)
    for idx, val in enumerate(list(first) + list(second)):
        out = jnp.where(lane == idx, val, out)
    return out


def _head_column(c, h):
    lane = lax.broadcasted_iota(jnp.int32, c.shape, 1)
    return jnp.sum(jnp.where(lane == h, c, 0.0), axis=1, keepdims=True)


ONES3 = [1.0, 1.0, 1.0]
ZEROS3 = [0.0, 0.0, 0.0]


def _fox_pack_fwd(qkv, c, *, name, tm=512):
    S = qkv.shape[0]

    def body(q_ref, k_ref, v_ref, c_ref, qo_ref, ko_ref, vo_ref):
        hp = pl.program_id(1)
        cv = c_ref[...]
        for hh in range(2):
            ch = _pieces(_head_column(cv, 2 * hp + hh))
            src = slice(hh * HEAD_DIM, (hh + 1) * HEAD_DIM)
            lo = slice(hh * PACK, hh * PACK + HEAD_DIM)
            hi = slice(hh * PACK + HEAD_DIM, (hh + 1) * PACK)
            qo_ref[:, lo] = (q_ref[:, src].astype(F32) * ATTN_SCALE).astype(qo_ref.dtype)
            qo_ref[:, hi] = _extras(ch, ONES3, tm).astype(qo_ref.dtype)
            ko_ref[:, lo] = k_ref[:, src]
            ko_ref[:, hi] = _extras(ONES3, [-p for p in ch], tm).astype(ko_ref.dtype)
            vo_ref[:, lo] = v_ref[:, src]
            vo_ref[:, hi] = _extras(ONES3, ZEROS3, tm).astype(vo_ref.dtype)

    def src(block0):
        return pl.BlockSpec((tm, PACK), lambda i, hp: (i, block0 + hp))

    out = pl.BlockSpec((tm, 2 * PACK), lambda i, hp: (i, hp))
    shp = jax.ShapeDtypeStruct((S, N_FOX_HEADS * PACK), _CD)
    return _pcall(body, name=name, grid=(S // tm, HEAD_PAIRS),
                  in_specs=[src(Q_BLOCK0), src(K_BLOCK0), src(V_BLOCK0), pl.BlockSpec((tm, PACK), lambda i, hp: (i, 0))],
                  out_specs=[out, out, out], out_shape=[shp, shp, shp],
                  compiler_params=_params("parallel", "parallel"))(qkv, qkv, qkv, c)


def _fox_fwd(qp, kp, vp, *, name):
    S = qp.shape[0]
    nt = S // FOX_T
    nt_dims = (((1,), (1,)), ((), ()))

    def body(q_ref, k_ref, v_ref, o_ref, l_ref, m_s, acc_s):
        i, j = pl.program_id(1), pl.program_id(2)

        @pl.when(j == 0)
        def _():
            m_s[...] = jnp.full((2, FOX_T, 1), NEG_INF, F32)
            acc_s[...] = jnp.zeros((2, FOX_T, PACK), F32)

        def tile(diagonal):
            for hh in range(2):
                cols = slice(hh * PACK, (hh + 1) * PACK)
                s = lax.dot_general(q_ref[:, cols], k_ref[:, cols], nt_dims, preferred_element_type=F32)
                if diagonal:
                    row = lax.broadcasted_iota(jnp.int32, (FOX_T, FOX_T), 0)
                    col = lax.broadcasted_iota(jnp.int32, (FOX_T, FOX_T), 1)
                    s = jnp.where(col <= row, s, NEG_INF)
                m_old = m_s[hh]
                m_new = jnp.maximum(m_old, jnp.max(s, axis=-1, keepdims=True))
                p = jnp.exp(s - m_new)
                acc_s[hh] = jnp.exp(m_old - m_new) * acc_s[hh] + jnp.dot(p.astype(_CD), v_ref[:, cols],
                                                                         preferred_element_type=F32)
                m_s[hh] = m_new

        @pl.when(j < i)
        def _():
            tile(False)

        @pl.when(j == i)
        def _():
            tile(True)

        @pl.when(j == nt - 1)
        def _():
            for hh in range(2):
                acc = acc_s[hh]
                den = acc[:, HEAD_DIM:HEAD_DIM + 1]
                cols = slice(hh * HEAD_DIM, (hh + 1) * HEAD_DIM)
                o_ref[:, cols] = acc[:, :HEAD_DIM] / den
                l_ref[:, cols] = jnp.broadcast_to(m_s[hh] + jnp.log(den), (FOX_T, HEAD_DIM))

    qs = pl.BlockSpec((FOX_T, 2 * PACK), lambda hp, i, j: (i, hp))
    ks = pl.BlockSpec((FOX_T, 2 * PACK), lambda hp, i, j: (jnp.minimum(i, j), hp))
    os_ = pl.BlockSpec((FOX_T, PACK), lambda hp, i, j: (i, hp))
    shp = jax.ShapeDtypeStruct((S, FOX_WIDTH), F32)
    return _pcall(body, name=name, grid=(HEAD_PAIRS, nt, nt), in_specs=[qs, ks, ks], out_specs=[os_, os_],
                  out_shape=[shp, shp],
                  scratch_shapes=[pltpu.VMEM((2, FOX_T, 1), F32), pltpu.VMEM((2, FOX_T, PACK), F32)],
                  compiler_params=_params("parallel", "parallel", "arbitrary"))(qp, kp, vp)


def _fox_pack_bwd(qkv, c, o, lse, do, *, name, tm=512):
    S = qkv.shape[0]

    def body(q_ref, c_ref, o_ref, l_ref, do_ref, qo_ref, do_out_ref):
        hp = pl.program_id(1)
        cv = c_ref[...]
        for hh in range(2):
            src = slice(hh * HEAD_DIM, (hh + 1) * HEAD_DIM)
            lo = slice(hh * PACK, hh * PACK + HEAD_DIM)
            hi = slice(hh * PACK + HEAD_DIM, (hh + 1) * PACK)
            shift = _head_column(cv, 2 * hp + hh) - l_ref[:, hh * HEAD_DIM:hh * HEAD_DIM + 1]
            dov = do_ref[:, src]
            dsum = jnp.sum(dov * o_ref[:, src], axis=-1, keepdims=True)
            qo_ref[:, lo] = (q_ref[:, src].astype(F32) * ATTN_SCALE).astype(qo_ref.dtype)
            qo_ref[:, hi] = _extras(_pieces(shift), ONES3, tm).astype(qo_ref.dtype)
            do_out_ref[:, lo] = dov.astype(do_out_ref.dtype)
            do_out_ref[:, hi] = _extras(_pieces(-dsum), ZEROS3, tm).astype(do_out_ref.dtype)

    pair = pl.BlockSpec((tm, PACK), lambda i, hp: (i, hp))
    out = pl.BlockSpec((tm, 2 * PACK), lambda i, hp: (i, hp))
    shp = jax.ShapeDtypeStruct((S, N_FOX_HEADS * PACK), _CD)
    return _pcall(body, name=name, grid=(S // tm, HEAD_PAIRS),
                  in_specs=[pl.BlockSpec((tm, PACK), lambda i, hp: (i, Q_BLOCK0 + hp)),
                            pl.BlockSpec((tm, PACK), lambda i, hp: (i, 0)), pair, pair, pair],
                  out_specs=[out, out], out_shape=[shp, shp],
                  compiler_params=_params("parallel", "parallel"))(qkv, c, o, lse, do)


def _fox_bwd(qp, kp, vp, dop, *, name):
    S = qp.shape[0]
    nt = S // FOX_T
    nt_dims = (((1,), (1,)), ((), ()))
    tn_dims = (((0,), (0,)), ((), ()))

    def body(q_ref, k_ref, v_ref, do_ref, dq_ref, dk_ref, dv_ref, dc_ref, dr_ref, dq_s, dk_s, dv_s, dc_s, dr_s):
        j, i = pl.program_id(1), pl.program_id(2)

        @pl.when(jnp.logical_and(j == 0, i == 0))
        def _():
            dq_s[...] = jnp.zeros((S, 2 * PACK), F32)
            dr_s[...] = jnp.zeros((2, 1, S), F32)

        @pl.when(i == 0)
        def _():
            dk_s[...] = jnp.zeros((FOX_T, 2 * PACK), F32)
            dv_s[...] = jnp.zeros((FOX_T, 2 * PACK), F32)
            dc_s[...] = jnp.zeros((2, FOX_T, 1), F32)

        def tile(diagonal):
            rows = pl.ds(pl.multiple_of(i * FOX_T, FOX_T), FOX_T)
            for hh in range(2):
                cols = slice(hh * PACK, (hh + 1) * PACK)
                qv, kv, vv, dov = q_ref[:, cols], k_ref[:, cols], v_ref[:, cols], do_ref[:, cols]
                pt = jnp.exp(lax.dot_general(kv, qv, nt_dims, preferred_element_type=F32))
                if diagonal:
                    key = lax.broadcasted_iota(jnp.int32, (FOX_T, FOX_T), 0)
                    qry = lax.broadcasted_iota(jnp.int32, (FOX_T, FOX_T), 1)
                    pt = jnp.where(key <= qry, pt, 0.0)
                dst = pt * lax.dot_general(vv, dov, nt_dims, preferred_element_type=F32)
                dsb = dst.astype(_CD)
                dc_s[hh] += jnp.sum(dst, axis=1, keepdims=True)
                dr_s[hh, :, rows] += jnp.sum(dst, axis=0, keepdims=True)
                dv_s[:, cols] += jnp.dot(pt.astype(_CD), dov, preferred_element_type=F32)
                dk_s[:, cols] += jnp.dot(dsb, qv, preferred_element_type=F32)
                dq_s[rows, cols] += lax.dot_general(dsb, kv, tn_dims, preferred_element_type=F32)

        @pl.when(i > j)
        def _():
            tile(False)

        @pl.when(i == j)
        def _():
            tile(True)

        @pl.when(i == nt - 1)
        def _():
            dk_ref[...] = dk_s[...].astype(dk_ref.dtype)
            dv_ref[...] = dv_s[...].astype(dv_ref.dtype)
            for hh in range(2):
                dc_ref[:, hh * HEAD_DIM:(hh + 1) * HEAD_DIM] = jnp.broadcast_to(dc_s[hh], (FOX_T, HEAD_DIM))

        @pl.when(jnp.logical_and(j == nt - 1, i == nt - 1))
        def _():
            lane = lax.broadcasted_iota(jnp.int32, (S, 2 * PACK), 1) % PACK
            dq_ref[...] = (dq_s[...] * jnp.where(lane < HEAD_DIM, ATTN_SCALE, 1.0)).astype(dq_ref.dtype)
            dr_ref[...] = dr_s[...]

    qs = pl.BlockSpec((FOX_T, 2 * PACK), lambda hp, j, i: (jnp.maximum(i, j), hp))
    ks = pl.BlockSpec((FOX_T, 2 * PACK), lambda hp, j, i: (j, hp))
    whole = pl.BlockSpec((S, 2 * PACK), lambda hp, j, i: (0, hp))
    cs = pl.BlockSpec((FOX_T, PACK), lambda hp, j, i: (j, hp))
    rs = pl.BlockSpec((2, 1, S), lambda hp, j, i: (hp, 0, 0))
    shp = jax.ShapeDtypeStruct((S, N_FOX_HEADS * PACK), _CD)
    return _pcall(body, name=name, grid=(HEAD_PAIRS, nt, nt), in_specs=[qs, ks, ks, qs],
                  out_specs=[whole, ks, ks, cs, rs],
                  out_shape=[shp, shp, shp, jax.ShapeDtypeStruct((S, FOX_WIDTH), F32),
                             jax.ShapeDtypeStruct((N_FOX_HEADS, 1, S), F32)],
                  scratch_shapes=[pltpu.VMEM((S, 2 * PACK), F32), pltpu.VMEM((FOX_T, 2 * PACK), F32),
                                  pltpu.VMEM((FOX_T, 2 * PACK), F32), pltpu.VMEM((2, FOX_T, 1), F32),
                                  pltpu.VMEM((2, 1, S), F32)],
                  compiler_params=_params("parallel", "arbitrary", "arbitrary"))(qp, kp, vp, dop)


def _dedilate(t, d):
    if d == 1:
        return t
    S, C = t.shape
    return t.reshape(S // d, d, C).transpose(1, 0, 2).reshape(S, C)


def _redilate(t, d):
    if d == 1:
        return t
    S, C = t.shape
    return t.reshape(d, S // d, C).transpose(1, 0, 2).reshape(S, C)


def _layer_step(x, tgt, w, p):
    S = x.shape[0]
    h = _rms_fwd(x, p["norm_mix_g"], name="rms_mix")
    qkv = _mm(h, w["qkv"], name="proj_qkv", out_dtype=_CD, tn=768)
    zf = _mm(h, w["f"], name="proj_f")
    gl = _mm(h, w["g"], name="proj_gate")

    dil_q, dil_k, dil_v = [], [], []
    dil_o, dil_l = [], []
    for g, (_, d) in enumerate(DIL_PAIRS):
        qg = _dedilate(qkv[:, g * DIL_OUT:(g + 1) * DIL_OUT], d)
        kg = _dedilate(qkv[:, DIL_WIDTH + g * DIL_OUT:DIL_WIDTH + (g + 1) * DIL_OUT], d)
        vg = _dedilate(qkv[:, 2 * DIL_WIDTH + g * DIL_OUT:2 * DIL_WIDTH + (g + 1) * DIL_OUT], d)
        og, lg = _dil_fwd(qg, kg, vg, g, name=f"dil_fwd{g}")
        dil_q.append(qg), dil_k.append(kg), dil_v.append(vg)
        dil_o.append(_redilate(og, d)), dil_l.append(_redilate(lg, d))
    o_a = _dil_mix_fwd(dil_o, dil_l, name="dil_mix")

    c = _fox_cumsum(zf, p["b_fgt"], name="fox_cumsum")
    fqp, fkp, fvp = _fox_pack_fwd(qkv, c, name="fox_pack")
    o_b, flse = _fox_fwd(fqp, fkp, fvp, name="fox_fwd")

    y_a = _mm(o_a, w["dil_out"], name="y_a")
    y_b = _mm(o_b, w["fox_out"], name="y_b")
    merged = _gate_fwd(gl, p["b_gate"], y_a, y_b, name="gate_fwd")
    x1 = _mm(merged, w["out"], name="mix_out", add=x)

    h2 = _rms_fwd(x1, p["norm_ffn_g"], name="rms_ffn")
    gu = _mm(h2, w["ffn_in"], name="ffn_in")
    act = _swiglu_fwd(gu, name="swiglu")
    x2 = _mm(act, w["ffn_down"], name="ffn_down", add=x1)

    loss, dx2, dg_final = _loss_head(x2, p["norm_final_g"], tgt, name="loss_head")

    dact = _mm(dx2, w["ffn_down"], name="d_act", tb=True, tn=1408)
    gw_ffn_down = _mm(act, dx2, name="gw_ffn_down", ta=True, out_dtype=_CD, tm=1408, tk=1024)
    dgu = _swiglu_bwd(dact, gu, name="swiglu_bwd")
    dh2 = _mm(dgu, w["ffn_in"], name="d_h2", tb=True, tk=1408)
    gw_ffn_in = _mm(h2, dgu, name="gw_ffn_in", ta=True, out_dtype=_CD, tk=1024)
    dx1, dg_ffn = _rms_bwd(x1, p["norm_ffn_g"], dh2, dx2, name="rms_ffn_bwd")

    dmerged = _mm(dx1, w["out"], name="d_merged", tb=True)
    gw_out = _mm(merged, dx1, name="gw_out", ta=True, out_dtype=_CD, tk=1024)
    dy_a, dy_b, dgl, db_gate = _gate_bwd(dmerged, gl, p["b_gate"], y_a, y_b, name="gate_bwd")
    do_a = _mm(dy_a, w["dil_out"], name="d_o_a", tb=True)
    gw_dil_out = _mm(o_a, dy_a, name="gw_dil_out", ta=True, out_dtype=_CD, tk=1024)
    do_b = _mm(dy_b, w["fox_out"], name="d_o_b", tb=True)
    gw_fox_out = _mm(o_b, dy_b, name="gw_fox_out", ta=True, out_dtype=_CD, tk=1024)

    bqp, bdop = _fox_pack_bwd(qkv, c, o_b, flse, do_b, name="fox_pack_bwd")
    dqp, dkp, dvp, dck, dcq = _fox_bwd(bqp, fkp, fvp, bdop, name="fox_bwd")
    dc = dcq[:, 0, :].T - dck.reshape(S, N_FOX_HEADS, HEAD_DIM)[:, :, 0]
    dc = jnp.pad(dc, ((0, 0), (0, F_PAD - N_FOX_HEADS)))
    dzf, db_fgt = _fox_cumsum_bwd(dc, zf, p["b_fgt"], name="fox_cumsum_bwd")

    def unpack(t):
        return t.reshape(S, N_FOX_HEADS, PACK)[:, :, :HEAD_DIM].reshape(S, FOX_WIDTH)

    douts = _dil_mix_bwd(do_a, dil_o, dil_l, name="dil_mix_bwd")
    dqs, dks, dvs = [], [], []
    for g, (_, d) in enumerate(DIL_PAIRS):
        dq, dk, dv = _dil_bwd(dil_q[g], dil_k[g], dil_v[g], _dedilate(dil_o[g], d), _dedilate(dil_l[g], d),
                              _dedilate(douts[g], d), _dedilate(douts[3 + g], d), g, name=f"dil_bwd{g}")
        dqs.append(_redilate(dq, d)), dks.append(_redilate(dk, d)), dvs.append(_redilate(dv, d))
    dqkv = jnp.concatenate(dqs + dks + dvs + [unpack(dqp), unpack(dkp), unpack(dvp)], axis=1)

    dh = _mm(dqkv, w["qkv"], name="d_h_qkv", tb=True, tk=1920)
    dh = _mm(dgl, w["g"], name="d_h_gate", tb=True, add=dh)
    dh = _mm(dzf, w["f"], name="d_h_f", tb=True, add=dh)
    gw_qkv = _mm(h, dqkv, name="gw_qkv", ta=True, out_dtype=_CD, tn=768, tk=1024)
    gw_g = _mm(h, dgl, name="gw_gate", ta=True, out_dtype=_CD, tk=1024)
    gw_f = _mm(h, dzf, name="gw_f", ta=True, out_dtype=_CD, tk=1024)
    dx, dg_mix = _rms_bwd(x, p["norm_mix_g"], dh, dx1, name="rms_mix_bwd")

    gw = dict(qkv=gw_qkv, f=gw_f, g=gw_g, dil_out=gw_dil_out, fox_out=gw_fox_out, out=gw_out, ffn_in=gw_ffn_in,
              ffn_down=gw_ffn_down)
    small = dict(norm_mix_g=dg_mix, b_fgt=db_fgt, b_gate=db_gate, norm_ffn_g=dg_ffn, norm_final_g=dg_final)
    return loss, dx, gw, small


def _position():
    return lax.axis_index("x"), lax.axis_index("y"), lax.axis_index("c")


def _other_chips(x, y):
    return [(1 - x, y), (x, 1 - y), (1 - x, 1 - y)]


def _gather_weights(shards):
    n = len(shards)

    def body(*refs):
        ins, outs = refs[:n], refs[n:2 * n]
        send_sems, recv_sems = refs[2 * n:]
        x, y, c = _position()
        mine = 2 * x + y
        chips = _other_chips(x, y)
        started = []
        for w in range(n):
            half = ins[w].shape[0] // 2
            rows = pl.ds(c * half, half)
            for r, (cx, cy) in enumerate(chips):
                cp = pltpu.make_async_remote_copy(
                    src_ref=ins[w].at[rows, :], dst_ref=outs[w].at[mine, rows, :], send_sem=send_sems.at[w, r],
                    recv_sem=recv_sems.at[w, r], device_id=(cx, cy, c), device_id_type=MESH)
                cp.start()
                started.append(cp)

        def landed(w, r, rows, peer):
            cx, cy = chips[r % 3]
            blk = outs[w].at[2 * cx + cy, rows, :]
            return pltpu.make_async_remote_copy(src_ref=blk, dst_ref=blk, send_sem=send_sems.at[w, r],
                                                recv_sem=recv_sems.at[w, r], device_id=peer, device_id_type=MESH)

        for w in range(n):
            half = ins[w].shape[0] // 2
            rows = pl.ds(c * half, half)
            for r, (cx, cy) in enumerate(chips):
                landed(w, r, rows, (cx, cy, c)).wait_recv()
                fwd = landed(w, 3 + r, rows, (x, y, 1 - c))
                fwd.start()
                started.append(fwd)
        for w in range(n):
            half = ins[w].shape[0] // 2
            rows = pl.ds((1 - c) * half, half)
            for r in range(3):
                landed(w, 3 + r, rows, (x, y, 1 - c)).wait_recv()
        for cp in started:
            cp.wait_send()

    return _pcall(
        body, name="gather_weights", in_specs=[HBM_SPEC] * n, out_specs=[HBM_SPEC] * n,
        out_shape=[jax.ShapeDtypeStruct((4,) + s.shape, s.dtype) for s in shards],
        scratch_shapes=[pltpu.SemaphoreType.DMA((n, 6)), pltpu.SemaphoreType.DMA((n, 6))],
    )(*shards)


def _swap_halves(grads):
    n = len(grads)

    def body(*refs):
        ins, outs = refs[:n], refs[n:2 * n]
        send_sems, recv_sems = refs[2 * n:]
        x, y, c = _position()
        copies = []
        for w in range(n):
            half = ins[w].shape[1] // 2
            cp = pltpu.make_async_remote_copy(
                src_ref=ins[w].at[:, pl.ds((1 - c) * half, half), :], dst_ref=outs[w], send_sem=send_sems.at[w],
                recv_sem=recv_sems.at[w], device_id=(x, y, 1 - c), device_id_type=MESH)
            cp.start()
            copies.append(cp)
        for cp in copies:
            cp.wait()

    return _pcall(
        body, name="swap_halves", in_specs=[HBM_SPEC] * n, out_specs=[HBM_SPEC] * n,
        out_shape=[jax.ShapeDtypeStruct((4, g.shape[1] // 2, g.shape[2]), g.dtype) for g in grads],
        scratch_shapes=[pltpu.SemaphoreType.DMA((n,)), pltpu.SemaphoreType.DMA((n,))],
    )(*grads)


def _scatter_to_owners(parts):
    n = len(parts)

    def body(*refs):
        ins, outs = refs[:n], refs[n:2 * n]
        send_sems, recv_sems = refs[2 * n:]
        x, y, c = _position()
        copies = []
        for w in range(n):
            for r, (cx, cy) in enumerate(_other_chips(x, y)):
                cp = pltpu.make_async_remote_copy(
                    src_ref=ins[w].at[2 * cx + cy], dst_ref=outs[w].at[r], send_sem=send_sems.at[w, r],
                    recv_sem=recv_sems.at[w, r], device_id=(cx, cy, c), device_id_type=MESH)
                cp.start()
                copies.append(cp)
        for cp in copies:
            cp.wait()

    return _pcall(
        body, name="scatter_to_owners", in_specs=[HBM_SPEC] * n, out_specs=[HBM_SPEC] * n,
        out_shape=[jax.ShapeDtypeStruct((3,) + p.shape[1:], p.dtype) for p in parts],
        scratch_shapes=[pltpu.SemaphoreType.DMA((n, 3)), pltpu.SemaphoreType.DMA((n, 3))],
    )(*parts)


def _share_halves(halves):
    n = len(halves)

    def body(*refs):
        ins, outs = refs[:n], refs[n:2 * n]
        send_sems, recv_sems = refs[2 * n:]
        x, y, c = _position()
        copies = []
        for w in range(n):
            cp = pltpu.make_async_remote_copy(src_ref=ins[w], dst_ref=outs[w], send_sem=send_sems.at[w],
                                              recv_sem=recv_sems.at[w], device_id=(x, y, 1 - c), device_id_type=MESH)
            cp.start()
            copies.append(cp)
        for cp in copies:
            cp.wait()

    return _pcall(
        body, name="share_halves", in_specs=[HBM_SPEC] * n, out_specs=[HBM_SPEC] * n,
        out_shape=[jax.ShapeDtypeStruct(h.shape, h.dtype) for h in halves],
        scratch_shapes=[pltpu.SemaphoreType.DMA((n,)), pltpu.SemaphoreType.DMA((n,))],
    )(*halves)


def _sum_small(part):
    rows, width = part.shape

    def body(x_ref, out_ref, all_ref, send_sems, recv_sems):
        x, y, c = _position()
        me, sibling = (x, y, c), (x, y, 1 - c)
        chips = _other_chips(x, y)

        def block(px, py, pc):
            return all_ref.at[pl.ds((4 * px + 2 * py + pc) * rows, rows), :]

        def copy(k, blk, to, src=None):
            return pltpu.make_async_remote_copy(
                src_ref=block(*blk) if src is None else src, dst_ref=block(*blk), send_sem=send_sems.at[k],
                recv_sem=recv_sems.at[k], device_id=to, device_id_type=MESH)

        all_ref[pl.ds((4 * x + 2 * y + c) * rows, rows), :] = x_ref[...]
        first = [copy(0, me, sibling, src=x_ref)]
        first += [copy(1 + j, me, (*chip, c), src=x_ref) for j, chip in enumerate(chips)]
        for cp in first:
            cp.start()
        passed = [copy(4 + j, (*chip, c), sibling) for j, chip in enumerate(chips)]
        for j, chip in enumerate(chips):
            copy(1 + j, (*chip, c), me).wait_recv()
            passed[j].start()
        copy(0, sibling, me).wait_recv()
        for j, chip in enumerate(chips):
            copy(4 + j, (*chip, 1 - c), me).wait_recv()
        for cp in first + passed:
            cp.wait_send()
        total = all_ref[0:rows, :]
        for d in range(1, 8):
            total = total + all_ref[d * rows:(d + 1) * rows, :]
        out_ref[...] = total

    vm = pl.BlockSpec(memory_space=pltpu.VMEM)
    return _pcall(
        body, name="sum_small", in_specs=[vm], out_specs=vm, out_shape=jax.ShapeDtypeStruct((rows, width), F32),
        scratch_shapes=[pltpu.VMEM((8 * rows, width), F32), pltpu.SemaphoreType.DMA((7,)), pltpu.SemaphoreType.DMA((7,))],
    )(part)


def _row_tile(R, C, itemsize=4, budget=1 << 20):
    for t in (512, 256, 128, 64, 32, 16, 8):
        if R % t == 0 and t * C * itemsize <= budget:
            return t
    return R


def _add_halves(g, recv, c, *, name):
    _, R, C = g.shape
    half = R // 2
    t = _row_tile(half, C)
    nb = half // t

    def body(c_ref, g_ref, r_ref, o_ref):
        o_ref[...] = (g_ref[...].astype(F32) + r_ref[...].astype(F32)).astype(o_ref.dtype)

    grid_spec = pltpu.PrefetchScalarGridSpec(
        num_scalar_prefetch=1, grid=(4, nb),
        in_specs=[pl.BlockSpec((1, t, C), lambda k, i, cr: (k, cr[0] * nb + i, 0)),
                  pl.BlockSpec((1, t, C), lambda k, i, cr: (k, i, 0))],
        out_specs=pl.BlockSpec((1, t, C), lambda k, i, cr: (k, i, 0)))
    return _pcall(body, name=name, grid_spec=grid_spec, out_shape=jax.ShapeDtypeStruct((4, half, C), g.dtype),
                  compiler_params=_params("parallel", "parallel"))(c, g, recv)


def _add_owners(mine, recv, *, name):
    half, C = mine.shape
    t = _row_tile(half, C)

    def body(m_ref, r_ref, o_ref):
        o_ref[...] = ((m_ref[...].astype(F32) + r_ref[0].astype(F32)) + r_ref[1].astype(F32)) + r_ref[2].astype(F32)

    return _pcall(body, name=name, grid=(half // t,),
                  in_specs=[pl.BlockSpec((t, C), lambda i: (i, 0)), pl.BlockSpec((3, t, C), lambda i: (0, i, 0))],
                  out_specs=pl.BlockSpec((t, C), lambda i: (i, 0)), out_shape=jax.ShapeDtypeStruct((half, C), F32),
                  compiler_params=_params("parallel"))(mine, recv)


def _adamw(w, g, m, v, *, name):
    R, C = w.shape
    t = _row_tile(R, C)
    c1 = 1.0 - ADAM_B1 ** ADAM_STEP
    c2 = 1.0 - ADAM_B2 ** ADAM_STEP

    def body(w_ref, g_ref, m_ref, v_ref, d_ref, nm_ref, nv_ref):
        gv = g_ref[...]
        mn = ADAM_B1 * m_ref[...] + (1.0 - ADAM_B1) * gv
        vn = ADAM_B2 * v_ref[...] + (1.0 - ADAM_B2) * (gv * gv)
        d_ref[...] = -ADAM_LR * ((mn / c1) / (jnp.sqrt(vn / c2) + ADAM_EPS) + ADAM_WD * w_ref[...])
        nm_ref[...] = mn
        nv_ref[...] = vn

    blk = pl.BlockSpec((t, C), lambda i: (i, 0))
    shp = jax.ShapeDtypeStruct((R, C), F32)
    return _pcall(body, name=name, grid=(R // t,), in_specs=[blk] * 4, out_specs=[blk] * 3, out_shape=[shp] * 3,
                  compiler_params=_params("parallel"))(w, g, m, v)


BIG = ("w_in", "w_dil_out", "w_fox_out", "w_out", "w_ffn_in", "w_ffn_down")
SMALL = ("norm_mix_g", "b_fgt", "b_gate", "norm_ffn_g", "norm_final_g")
ORDER = ("norm_mix_g", "w_in", "b_fgt", "b_gate", "w_dil_out", "w_fox_out", "w_out", "norm_ffn_g", "w_ffn_in",
         "w_ffn_down", "norm_final_g")
SMALL_ROWS = {"norm_mix_g": (0, 1), "b_gate": (1, 3), "norm_ffn_g": (3, 4), "norm_final_g": (4, 5), "b_fgt": (5, 6)}


def _columns_to_blocks(full, ncol):
    K = full.shape[0]
    return full.reshape(K, 4, ncol).transpose(1, 0, 2)


def _blocks_to_columns(blocks):
    n, K, ncol = blocks.shape
    return blocks.transpose(1, 0, 2).reshape(K, n * ncol)


def kernel(x, norm_mix_g, w_in, b_fgt, b_gate, w_dil_out, w_fox_out, w_out, norm_ffn_g, w_ffn_in, w_ffn_down, norm_final_g, loss_target, m_norm_mix_g, m_w_in, m_b_fgt, m_b_gate, m_w_dil_out, m_w_fox_out, m_w_out, m_norm_ffn_g, m_w_ffn_in, m_w_ffn_down, m_norm_final_g, v_norm_mix_g, v_w_in, v_b_fgt, v_b_gate, v_w_dil_out, v_w_fox_out, v_w_out, v_norm_ffn_g, v_w_ffn_in, v_w_ffn_down, v_norm_final_g):
    weights = dict(norm_mix_g=norm_mix_g, w_in=w_in, b_fgt=b_fgt, b_gate=b_gate, w_dil_out=w_dil_out,
                   w_fox_out=w_fox_out, w_out=w_out, norm_ffn_g=norm_ffn_g, w_ffn_in=w_ffn_in, w_ffn_down=w_ffn_down,
                   norm_final_g=norm_final_g)
    m_in = dict(norm_mix_g=m_norm_mix_g, w_in=m_w_in, b_fgt=m_b_fgt, b_gate=m_b_gate, w_dil_out=m_w_dil_out,
                w_fox_out=m_w_fox_out, w_out=m_w_out, norm_ffn_g=m_norm_ffn_g, w_ffn_in=m_w_ffn_in,
                w_ffn_down=m_w_ffn_down, norm_final_g=m_norm_final_g)
    v_in = dict(norm_mix_g=v_norm_mix_g, w_in=v_w_in, b_fgt=v_b_fgt, b_gate=v_b_gate, w_dil_out=v_w_dil_out,
                w_fox_out=v_w_fox_out, w_out=v_w_out, norm_ffn_g=v_norm_ffn_g, w_ffn_in=v_w_ffn_in,
                w_ffn_down=v_w_ffn_down, norm_final_g=v_norm_final_g)
    c = lax.axis_index("c")
    chip = 2 * lax.axis_index("x") + lax.axis_index("y")

    shards = [weights[n][0].astype(_CD) for n in BIG]
    gathered = _gather_weights(shards)
    g_in, g_dil, g_fox, g_out, g_ffn_in, g_ffn_down = [
        lax.dynamic_update_index_in_dim(g, s, chip, 0) for g, s in zip(gathered, shards)]
    full_in = _blocks_to_columns(g_in)
    o3 = QKV_COLS
    o4 = o3 + N_FOX_HEADS
    w = dict(
        qkv=full_in[:, :o3],
        f=jnp.pad(full_in[:, o3:o4], ((0, 0), (0, F_PAD - N_FOX_HEADS))),
        g=full_in[:, o4:],
        dil_out=_blocks_to_columns(g_dil), fox_out=_blocks_to_columns(g_fox),
        out=g_out.reshape(D_MODEL, D_MODEL), ffn_in=_blocks_to_columns(g_ffn_in),
        ffn_down=g_ffn_down.reshape(D_FF, D_MODEL))
    p = dict(norm_mix_g=norm_mix_g, b_fgt=jnp.pad(b_fgt, ((0, 0), (0, F_PAD - N_FOX_HEADS))), b_gate=b_gate,
             norm_ffn_g=norm_ffn_g, norm_final_g=norm_final_g.reshape(1, D_MODEL))

    loss_part, grad_x, gw, small = _layer_step(x[0], loss_target[0], w, p)

    gw_in = jnp.concatenate([gw["qkv"], gw["f"][:, :N_FOX_HEADS], gw["g"]], axis=1)
    blocks = [
        _columns_to_blocks(gw_in, w_in.shape[2]),
        _columns_to_blocks(gw["dil_out"], w_dil_out.shape[2]),
        _columns_to_blocks(gw["fox_out"], w_fox_out.shape[2]),
        gw["out"].reshape(4, w_out.shape[1], D_MODEL),
        _columns_to_blocks(gw["ffn_in"], w_ffn_in.shape[2]),
        gw["ffn_down"].reshape(4, w_ffn_down.shape[1], D_MODEL),
    ]
    from_sibling = _swap_halves(blocks)
    c_arr = jnp.reshape(c, (1,)).astype(jnp.int32)
    chip_sums = [_add_halves(b, r, c_arr, name=f"add_halves_{n}") for b, r, n in zip(blocks, from_sibling, BIG)]
    from_chips = _scatter_to_owners(chip_sums)
    halves = [_add_owners(lax.dynamic_index_in_dim(s, chip, 0, keepdims=False), r, name=f"add_owners_{n}")
              for s, r, n in zip(chip_sums, from_chips, BIG)]
    grads = {}
    for n, own, other in zip(BIG, halves, _share_halves(halves)):
        pair = jnp.stack([own, other])
        grads[n] = jnp.where(c == 0, pair, pair[::-1]).reshape(2 * own.shape[0], own.shape[1])

    packed = jnp.concatenate([
        small["norm_mix_g"], small["b_gate"].reshape(2, D_MODEL), small["norm_ffn_g"], small["norm_final_g"],
        jnp.pad(small["b_fgt"], ((0, 0), (0, D_MODEL - F_PAD))), jnp.zeros((2, D_MODEL), F32)], axis=0)
    summed = _sum_small(packed)
    for n in SMALL:
        lo, hi = SMALL_ROWS[n]
        grads[n] = summed[lo:hi].reshape(1, -1)[:, :weights[n].size]

    loss = lax.psum(loss_part[0, 0], ("x", "y", "c"))

    out_g, out_d, out_m, out_v = {}, {}, {}, {}
    for n in ORDER:
        shape = weights[n].shape
        two_d = shape[1:] if len(shape) == 3 else (1, weights[n].size)
        g2 = grads[n].reshape(two_d)
        d2, m2, v2 = _adamw(weights[n].reshape(two_d), g2, m_in[n].reshape(two_d), v_in[n].reshape(two_d),
                            name=f"adamw_{n}")
        out_g[n], out_d[n], out_m[n], out_v[n] = (g2.reshape(shape), d2.reshape(shape), m2.reshape(shape),
                                                  v2.reshape(shape))
    return (loss, grad_x[None], *[out_g[n] for n in ORDER], *[out_d[n] for n in ORDER],
            *[out_m[n] for n in ORDER], *[out_v[n] for n in ORDER])
```

```python
import numpy as np
import jax
import jax.numpy as jnp
from jax import lax
from jax.experimental import pallas as pl
from jax.experimental.pallas import tpu as pltpu

F32 = jnp.float32
_CD = jnp.bfloat16

D_MODEL = 1024
HEAD_DIM = 64
DIL_PAIRS = ((128, 1), (512, 4), (2048, 16))
N_DIL_GROUPS = 3
DIL_HEADS = 4
DIL_W = 128
DIL_OUT = DIL_HEADS * HEAD_DIM
DIL_WIDTH = N_DIL_GROUPS * DIL_OUT
N_FOX_HEADS = 8
FOX_WIDTH = N_FOX_HEADS * HEAD_DIM
D_FF = 2816
QKV_COLS = 3 * DIL_WIDTH + 3 * FOX_WIDTH
F_PAD = 128
RMS_EPS = 1e-6
NEG_INF = -1e30
ATTN_SCALE = HEAD_DIM ** -0.5
ADAM_LR, ADAM_B1, ADAM_B2, ADAM_EPS, ADAM_WD, ADAM_STEP = 0.001, 0.9, 0.999, 1e-08, 0.01, 10

VMEM_LIMIT = 48 * 1024 * 1024
MESH = pl.DeviceIdType.MESH
HBM_SPEC = pl.BlockSpec(memory_space=pltpu.HBM)


def _pcall(body, **kw):
    return pl.pallas_call(body, **kw)


def _params(*sem):
    return pltpu.CompilerParams(dimension_semantics=sem, vmem_limit_bytes=VMEM_LIMIT)


def _pick(dim, pref):
    t = (min(pref, dim) // 128) * 128
    while t >= 128:
        if dim % t == 0:
            return t
        t -= 128
    return dim


def _mm(a, b, *, name, ta=False, tb=False, out_dtype=F32, add=None, tm=1024, tn=512, tk=2048):
    if ta:
        K, M = a.shape
    else:
        M, K = a.shape
    if tb:
        N, K2 = b.shape
    else:
        K2, N = b.shape
    assert K == K2, (a.shape, b.shape)
    tm, tn, tk = _pick(M, tm), _pick(N, tn), _pick(K, tk)
    nk = K // tk
    dn = (((0 if ta else 1,), (1 if tb else 0,)), ((), ()))
    has_add = add is not None

    def body(*refs):
        a_ref, b_ref = refs[0], refs[1]
        add_ref = refs[2] if has_add else None
        o_ref = refs[3] if has_add else refs[2]
        p = lax.dot_general(a_ref[...].astype(_CD), b_ref[...].astype(_CD), dn, preferred_element_type=F32)

        def finish(r):
            if has_add:
                r = r + add_ref[...]
            o_ref[...] = r.astype(out_dtype)

        if nk == 1:
            finish(p)
        else:
            acc_ref = refs[-1]
            k = pl.program_id(2)

            @pl.when(k == 0)
            def _():
                acc_ref[...] = p

            @pl.when(k > 0)
            def _():
                acc_ref[...] += p

            @pl.when(k == nk - 1)
            def _():
                finish(acc_ref[...])

    a_spec = pl.BlockSpec((tk, tm), lambda i, j, k: (k, i)) if ta else pl.BlockSpec((tm, tk), lambda i, j, k: (i, k))
    b_spec = pl.BlockSpec((tn, tk), lambda i, j, k: (j, k)) if tb else pl.BlockSpec((tk, tn), lambda i, j, k: (k, j))
    o_spec = pl.BlockSpec((tm, tn), lambda i, j, k: (i, j))
    in_specs = [a_spec, b_spec] + ([o_spec] if has_add else [])
    args = (a, b) + ((add,) if has_add else ())
    return _pcall(
        body, name=name, grid=(M // tm, N // tn, nk), in_specs=in_specs, out_specs=o_spec,
        out_shape=jax.ShapeDtypeStruct((M, N), out_dtype),
        scratch_shapes=[pltpu.VMEM((tm, tn), F32)] if nk > 1 else [],
        compiler_params=_params("parallel", "parallel", "arbitrary"),
    )(*args)


def _rms_fwd(x, g, *, name, tm=512):
    S, D = x.shape

    def body(x_ref, g_ref, h_ref):
        xv = x_ref[...]
        r = lax.rsqrt(jnp.mean(xv * xv, axis=-1, keepdims=True) + RMS_EPS)
        h_ref[...] = ((xv * r) * g_ref[...]).astype(h_ref.dtype)

    row = pl.BlockSpec((tm, D), lambda i: (i, 0))
    return _pcall(body, name=name, grid=(S // tm,), in_specs=[row, pl.BlockSpec((1, D), lambda i: (0, 0))],
                  out_specs=row, out_shape=jax.ShapeDtypeStruct((S, D), _CD), compiler_params=_params("parallel"))(x, g)


def _rms_bwd(x, g, dh, dres, *, name, tm=512):
    S, D = x.shape

    def body(x_ref, g_ref, dh_ref, dres_ref, dx_ref, dg_ref):
        xv = x_ref[...]
        r = lax.rsqrt(jnp.mean(xv * xv, axis=-1, keepdims=True) + RMS_EPS)
        xh = xv * r
        dhv = dh_ref[...]
        dxh = dhv * g_ref[...]
        dx_ref[...] = dres_ref[...] + r * (dxh - xh * jnp.mean(dxh * xh, axis=-1, keepdims=True))
        part = jnp.sum(dhv * xh, axis=0, keepdims=True)

        @pl.when(pl.program_id(0) == 0)
        def _():
            dg_ref[...] = part

        @pl.when(pl.program_id(0) > 0)
        def _():
            dg_ref[...] += part

    row = pl.BlockSpec((tm, D), lambda i: (i, 0))
    vec = pl.BlockSpec((1, D), lambda i: (0, 0))
    return _pcall(body, name=name, grid=(S // tm,), in_specs=[row, vec, row, row], out_specs=[row, vec],
                  out_shape=[jax.ShapeDtypeStruct((S, D), F32), jax.ShapeDtypeStruct((1, D), F32)],
                  compiler_params=_params("arbitrary"))(x, g, dh, dres)


def _loss_head(x, g, tgt, *, name, tm=512):
    S, D = x.shape

    def body(x_ref, g_ref, t_ref, loss_ref, dx_ref, dg_ref):
        xv = x_ref[...]
        gv = g_ref[...]
        r = lax.rsqrt(jnp.mean(xv * xv, axis=-1, keepdims=True) + RMS_EPS)
        xh = xv * r
        err = xh * gv - t_ref[...]
        lpart = 0.5 * jnp.sum(jnp.mean(err * err, axis=-1, keepdims=True), axis=0, keepdims=True)
        dy = err * (1.0 / D)
        dxh = dy * gv
        dx_ref[...] = r * (dxh - xh * jnp.mean(dxh * xh, axis=-1, keepdims=True))
        gpart = jnp.sum(dy * xh, axis=0, keepdims=True)

        @pl.when(pl.program_id(0) == 0)
        def _():
            loss_ref[...] = lpart
            dg_ref[...] = gpart

        @pl.when(pl.program_id(0) > 0)
        def _():
            loss_ref[...] += lpart
            dg_ref[...] += gpart

    row = pl.BlockSpec((tm, D), lambda i: (i, 0))
    vec = pl.BlockSpec((1, D), lambda i: (0, 0))
    one = pl.BlockSpec((1, 1), lambda i: (0, 0))
    return _pcall(body, name=name, grid=(S // tm,), in_specs=[row, vec, row], out_specs=[one, row, vec],
                  out_shape=[jax.ShapeDtypeStruct((1, 1), F32), jax.ShapeDtypeStruct((S, D), F32),
                             jax.ShapeDtypeStruct((1, D), F32)],
                  compiler_params=_params("arbitrary"))(x, g, tgt)


def _sigmoid(z):
    return 1.0 / (1.0 + jnp.exp(-z))


def _gate_fwd(gl, bg, ya, yb, *, name, tm=512):
    S, D = ya.shape

    def body(za_ref, zb_ref, ba_ref, bb_ref, ya_ref, yb_ref, o_ref):
        ga = _sigmoid(za_ref[...] + ba_ref[...])
        gb = _sigmoid(zb_ref[...] + bb_ref[...])
        o_ref[...] = (ga * ya_ref[...] + gb * yb_ref[...]).astype(o_ref.dtype)

    lo = pl.BlockSpec((tm, D), lambda i: (i, 0))
    hi = pl.BlockSpec((tm, D), lambda i: (i, 1))
    vlo = pl.BlockSpec((1, D), lambda i: (0, 0))
    vhi = pl.BlockSpec((1, D), lambda i: (0, 1))
    return _pcall(body, name=name, grid=(S // tm,), in_specs=[lo, hi, vlo, vhi, lo, lo], out_specs=lo,
                  out_shape=jax.ShapeDtypeStruct((S, D), _CD), compiler_params=_params("parallel"))(gl, gl, bg, bg, ya, yb)


def _gate_bwd(dm, gl, bg, ya, yb, *, name, tm=256):
    S, D = ya.shape

    def body(dm_ref, za_ref, zb_ref, ba_ref, bb_ref, ya_ref, yb_ref, dya_ref, dyb_ref, dgl_ref, dbg_ref):
        dmv = dm_ref[...]
        ga = _sigmoid(za_ref[...] + ba_ref[...])
        gb = _sigmoid(zb_ref[...] + bb_ref[...])
        dya_ref[...] = (dmv * ga).astype(dya_ref.dtype)
        dyb_ref[...] = (dmv * gb).astype(dyb_ref.dtype)
        dza = dmv * ya_ref[...] * ga * (1.0 - ga)
        dzb = dmv * yb_ref[...] * gb * (1.0 - gb)
        dgl_ref[:, :D] = dza.astype(dgl_ref.dtype)
        dgl_ref[:, D:] = dzb.astype(dgl_ref.dtype)
        pa = jnp.sum(dza, axis=0, keepdims=True)
        pb = jnp.sum(dzb, axis=0, keepdims=True)

        @pl.when(pl.program_id(0) == 0)
        def _():
            dbg_ref[:, :D] = pa
            dbg_ref[:, D:] = pb

        @pl.when(pl.program_id(0) > 0)
        def _():
            dbg_ref[:, :D] += pa
            dbg_ref[:, D:] += pb

    lo = pl.BlockSpec((tm, D), lambda i: (i, 0))
    hi = pl.BlockSpec((tm, D), lambda i: (i, 1))
    vlo = pl.BlockSpec((1, D), lambda i: (0, 0))
    vhi = pl.BlockSpec((1, D), lambda i: (0, 1))
    wide = pl.BlockSpec((tm, 2 * D), lambda i: (i, 0))
    vwide = pl.BlockSpec((1, 2 * D), lambda i: (0, 0))
    return _pcall(body, name=name, grid=(S // tm,), in_specs=[lo, lo, hi, vlo, vhi, lo, lo],
                  out_specs=[lo, lo, wide, vwide],
                  out_shape=[jax.ShapeDtypeStruct((S, D), _CD), jax.ShapeDtypeStruct((S, D), _CD),
                             jax.ShapeDtypeStruct((S, 2 * D), _CD), jax.ShapeDtypeStruct((1, 2 * D), F32)],
                  compiler_params=_params("arbitrary"))(dm, gl, gl, bg, bg, ya, yb)


def _swiglu_fwd(gu, *, name, tm=256):
    S, F2 = gu.shape
    F = F2 // 2

    def body(g_ref, u_ref, o_ref):
        gv = g_ref[...]
        o_ref[...] = (gv * _sigmoid(gv) * u_ref[...]).astype(o_ref.dtype)

    lo = pl.BlockSpec((tm, F), lambda i: (i, 0))
    hi = pl.BlockSpec((tm, F), lambda i: (i, 1))
    return _pcall(body, name=name, grid=(S // tm,), in_specs=[lo, hi], out_specs=lo,
                  out_shape=jax.ShapeDtypeStruct((S, F), _CD), compiler_params=_params("parallel"))(gu, gu)


def _swiglu_bwd(dact, gu, *, name, tm=256):
    S, F2 = gu.shape
    F = F2 // 2

    def body(d_ref, g_ref, u_ref, o_ref):
        dv = d_ref[...]
        gv = g_ref[...]
        sg = _sigmoid(gv)
        o_ref[:, :F] = (dv * u_ref[...] * (sg * (1.0 + gv * (1.0 - sg)))).astype(o_ref.dtype)
        o_ref[:, F:] = (dv * (gv * sg)).astype(o_ref.dtype)

    lo = pl.BlockSpec((tm, F), lambda i: (i, 0))
    hi = pl.BlockSpec((tm, F), lambda i: (i, 1))
    return _pcall(body, name=name, grid=(S // tm,), in_specs=[lo, lo, hi],
                  out_specs=pl.BlockSpec((tm, F2), lambda i: (i, 0)),
                  out_shape=jax.ShapeDtypeStruct((S, F2), _CD), compiler_params=_params("parallel"))(dact, gu, gu)


def _split3(x):
    hi = x.astype(jnp.bfloat16)
    r1 = x - hi.astype(F32)
    mid = r1.astype(jnp.bfloat16)
    lo = (r1 - mid.astype(F32)).astype(jnp.bfloat16)
    return hi, mid, lo


def _ones_dot_left(ones, x):
    return sum(jnp.dot(ones, p, preferred_element_type=F32) for p in _split3(x))


def _ones_dot_right(x, ones):
    return sum(jnp.dot(p, ones, preferred_element_type=F32) for p in _split3(x))


def _head_sum(x):
    n = x.shape[1]
    r = lax.broadcasted_iota(jnp.int32, (n, n), 0) // HEAD_DIM
    c = lax.broadcasted_iota(jnp.int32, (n, n), 1) // HEAD_DIM
    return _ones_dot_right(x, (r == c).astype(jnp.bfloat16))


def _log_sigmoid(z):
    e = jnp.exp(-jnp.abs(z))
    t = 1.0 + e
    log1p_e = jnp.where(t == 1.0, e, jnp.log(t) * (e / jnp.where(t == 1.0, 1.0, t - 1.0)))
    return jnp.minimum(z, 0.0) - log1p_e


def _fox_cumsum(zf, bf, *, name):
    S, W = zf.shape
    nb = S // 128

    def body(z_ref, b_ref, c_ref):
        tri = (lax.broadcasted_iota(jnp.int32, (128, 128), 0) >= lax.broadcasted_iota(jnp.int32, (128, 128), 1))
        tri = tri.astype(jnp.bfloat16)

        def step(i, carry):
            rows = pl.ds(pl.multiple_of(i * 128, 128), 128)
            lf = _log_sigmoid(z_ref[rows, :] + b_ref[...])
            cb = _ones_dot_left(tri, lf) + carry
            c_ref[rows, :] = cb
            return cb[127:128, :]

        lax.fori_loop(0, nb, step, jnp.zeros((1, W), F32))

    return _pcall(body, name=name, out_shape=jax.ShapeDtypeStruct((S, W), F32),
                  compiler_params=pltpu.CompilerParams(vmem_limit_bytes=VMEM_LIMIT))(zf, bf)


def _fox_cumsum_bwd(dc, zf, bf, *, name):
    S, W = zf.shape
    nb = S // 128

    def body(dc_ref, z_ref, b_ref, dz_ref, db_ref):
        tri = (lax.broadcasted_iota(jnp.int32, (128, 128), 0) <= lax.broadcasted_iota(jnp.int32, (128, 128), 1))
        tri = tri.astype(jnp.bfloat16)

        def step(k, carry):
            tail, acc = carry
            i = nb - 1 - k
            rows = pl.ds(pl.multiple_of(i * 128, 128), 128)
            dlf = _ones_dot_left(tri, dc_ref[rows, :]) + tail
            dz = dlf * _sigmoid(-(z_ref[rows, :] + b_ref[...]))
            dz_ref[rows, :] = dz
            return dlf[0:1, :], acc + jnp.sum(dz, axis=0, keepdims=True)

        _, acc = lax.fori_loop(0, nb, step, (jnp.zeros((1, W), F32), jnp.zeros((1, W), F32)))
        db_ref[...] = acc

    return _pcall(body, name=name,
                  out_shape=[jax.ShapeDtypeStruct((S, W), F32), jax.ShapeDtypeStruct((1, W), F32)],
                  compiler_params=pltpu.CompilerParams(vmem_limit_bytes=VMEM_LIMIT))(dc, zf, bf)


def _dil_slopes(group):
    h = np.arange(1, N_DIL_GROUPS * DIL_HEADS + 1, dtype=np.float32)
    s = (np.float32(2.0) ** (np.float32(-8.0) * h / np.float32(N_DIL_GROUPS * DIL_HEADS))).astype(np.float32)
    return [float(v) for v in s.reshape(N_DIL_GROUPS, DIL_HEADS)[group]]


def _dil_tiles(i, n, blocks_per_seq):
    qi = lax.broadcasted_iota(jnp.int32, (DIL_W, DIL_W), 0)
    kj = lax.broadcasted_iota(jnp.int32, (DIL_W, DIL_W), 1)
    first = ((4 * n + i) % blocks_per_seq) == 0
    valid_prev = jnp.logical_and(kj >= qi, jnp.logical_not(first))
    valid_cur = kj <= qi
    rel_prev = (qi - kj + DIL_W).astype(F32)
    rel_cur = (qi - kj).astype(F32)
    return valid_prev, valid_cur, rel_prev, rel_cur


CHUNK = 4 * DIL_W


def _dil_fwd(q, k, v, group, *, name):
    S = q.shape[0]
    dilation = DIL_PAIRS[group][1]
    bps = (S // dilation) // DIL_W
    slopes = _dil_slopes(group)
    nt = (((1,), (1,)), ((), ()))

    def body(q_ref, k_ref, v_ref, kp_ref, vp_ref, o_ref, l_ref):
        n = pl.program_id(0)
        for i in range(4):
            valid_prev, valid_cur, rel_prev, rel_cur = _dil_tiles(i, n, bps)
            rows = slice(i * DIL_W, (i + 1) * DIL_W)
            prow = slice((i - 1) * DIL_W, i * DIL_W)
            for h in range(DIL_HEADS):
                cols = slice(h * HEAD_DIM, (h + 1) * HEAD_DIM)
                qh = q_ref[rows, cols]
                kc, vc = k_ref[rows, cols], v_ref[rows, cols]
                kp = kp_ref[:, cols] if i == 0 else k_ref[prow, cols]
                vp = vp_ref[:, cols] if i == 0 else v_ref[prow, cols]
                sl = slopes[h] * dilation
                sp = lax.dot_general(qh, kp, nt, preferred_element_type=F32) * ATTN_SCALE - sl * rel_prev
                sc = lax.dot_general(qh, kc, nt, preferred_element_type=F32) * ATTN_SCALE - sl * rel_cur
                sp = jnp.where(valid_prev, sp, NEG_INF)
                sc = jnp.where(valid_cur, sc, NEG_INF)
                m = jnp.maximum(jnp.max(sp, axis=-1, keepdims=True), jnp.max(sc, axis=-1, keepdims=True))
                pp, pc = jnp.exp(sp - m), jnp.exp(sc - m)
                den = jnp.sum(pp, axis=-1, keepdims=True) + jnp.sum(pc, axis=-1, keepdims=True)
                acc = (jnp.dot(pp.astype(_CD), vp, preferred_element_type=F32)
                       + jnp.dot(pc.astype(_CD), vc, preferred_element_type=F32))
                o_ref[rows, cols] = acc / den
                l_ref[rows, cols] = jnp.broadcast_to(m + jnp.log(den), (DIL_W, HEAD_DIM))

    cur = pl.BlockSpec((CHUNK, DIL_OUT), lambda n: (n, 0))
    prev = pl.BlockSpec((DIL_W, DIL_OUT), lambda n: (jnp.maximum(4 * n - 1, 0), 0))
    return _pcall(body, name=name, grid=(S // CHUNK,), in_specs=[cur, cur, cur, prev, prev], out_specs=[cur, cur],
                  out_shape=[jax.ShapeDtypeStruct((S, DIL_OUT), F32), jax.ShapeDtypeStruct((S, DIL_OUT), F32)],
                  compiler_params=_params("parallel"))(q, k, v, k, v)


def _dil_bwd(q, k, v, o, lse, do, dlse, group, *, name):
    S = q.shape[0]
    dilation = DIL_PAIRS[group][1]
    bps = (S // dilation) // DIL_W
    slopes = _dil_slopes(group)
    nchunk = S // CHUNK
    nt = (((1,), (1,)), ((), ()))
    tn = (((0,), (0,)), ((), ()))

    def body(q_ref, k_ref, v_ref, kp_ref, vp_ref, o_ref, l_ref, do_ref, dl_ref, dq_ref, dk_ref, dv_ref,
             dk_s, dv_s):
        step = pl.program_id(0)
        n = nchunk - 1 - step

        @pl.when(step == 0)
        def _():
            dk_s[CHUNK:, :] = jnp.zeros((DIL_W, DIL_OUT), F32)
            dv_s[CHUNK:, :] = jnp.zeros((DIL_W, DIL_OUT), F32)

        dk_s[:CHUNK, :] = jnp.zeros((CHUNK, DIL_OUT), F32)
        dv_s[:CHUNK, :] = jnp.zeros((CHUNK, DIL_OUT), F32)
        for i in range(4):
            valid_prev, valid_cur, rel_prev, rel_cur = _dil_tiles(i, n, bps)
            rows = slice(i * DIL_W, (i + 1) * DIL_W)
            prow = slice((i - 1) * DIL_W, i * DIL_W)
            s_prev = slice(i * DIL_W, (i + 1) * DIL_W)
            s_cur = slice((i + 1) * DIL_W, (i + 2) * DIL_W)
            for h in range(DIL_HEADS):
                cols = slice(h * HEAD_DIM, (h + 1) * HEAD_DIM)
                qh = q_ref[rows, cols]
                kc, vc = k_ref[rows, cols], v_ref[rows, cols]
                kp = kp_ref[:, cols] if i == 0 else k_ref[prow, cols]
                vp = vp_ref[:, cols] if i == 0 else v_ref[prow, cols]
                sl = slopes[h] * dilation
                lh = l_ref[rows, h * HEAD_DIM:h * HEAD_DIM + 1]
                sp = lax.dot_general(qh, kp, nt, preferred_element_type=F32) * ATTN_SCALE - sl * rel_prev
                sc = lax.dot_general(qh, kc, nt, preferred_element_type=F32) * ATTN_SCALE - sl * rel_cur
                pp = jnp.exp(jnp.where(valid_prev, sp, NEG_INF) - lh)
                pc = jnp.exp(jnp.where(valid_cur, sc, NEG_INF) - lh)
                doh = do_ref[rows, cols]
                dsum = jnp.sum(doh * o_ref[rows, cols], axis=-1, keepdims=True)
                shift = dl_ref[rows, h * HEAD_DIM:h * HEAD_DIM + 1] - dsum
                dob = doh.astype(_CD)
                dsp = pp * (lax.dot_general(dob, vp, nt, preferred_element_type=F32) + shift)
                dsc = pc * (lax.dot_general(dob, vc, nt, preferred_element_type=F32) + shift)
                dspb = (dsp * ATTN_SCALE).astype(_CD)
                dscb = (dsc * ATTN_SCALE).astype(_CD)
                dq_ref[rows, cols] = (jnp.dot(dspb, kp, preferred_element_type=F32)
                                      + jnp.dot(dscb, kc, preferred_element_type=F32)).astype(dq_ref.dtype)
                dk_s[s_prev, cols] += lax.dot_general(dspb, qh, tn, preferred_element_type=F32)
                dk_s[s_cur, cols] += lax.dot_general(dscb, qh, tn, preferred_element_type=F32)
                dv_s[s_prev, cols] += lax.dot_general(pp.astype(_CD), dob, tn, preferred_element_type=F32)
                dv_s[s_cur, cols] += lax.dot_general(pc.astype(_CD), dob, tn, preferred_element_type=F32)
        dk_ref[...] = dk_s[DIL_W:, :].astype(dk_ref.dtype)
        dv_ref[...] = dv_s[DIL_W:, :].astype(dv_ref.dtype)
        dk_s[CHUNK:, :] = dk_s[:DIL_W, :]
        dv_s[CHUNK:, :] = dv_s[:DIL_W, :]

    cur = pl.BlockSpec((CHUNK, DIL_OUT), lambda s: (nchunk - 1 - s, 0))
    prev = pl.BlockSpec((DIL_W, DIL_OUT), lambda s: (jnp.maximum(4 * (nchunk - 1 - s) - 1, 0), 0))
    shp = jax.ShapeDtypeStruct((S, DIL_OUT), _CD)
    return _pcall(body, name=name, grid=(nchunk,), in_specs=[cur, cur, cur, prev, prev, cur, cur, cur, cur],
                  out_specs=[cur, cur, cur], out_shape=[shp, shp, shp],
                  scratch_shapes=[pltpu.VMEM((CHUNK + DIL_W, DIL_OUT), F32), pltpu.VMEM((CHUNK + DIL_W, DIL_OUT), F32)],
                  compiler_params=_params("arbitrary"))(q, k, v, k, v, o, lse, do, dlse)


def _dil_mix_fwd(os_, ls_, *, name, tm=512):
    S, W = os_[0].shape

    def body(o0, o1, o2, l0, l1, l2, out_ref):
        ls = [l0[...], l1[...], l2[...]]
        m = jnp.maximum(jnp.maximum(ls[0], ls[1]), ls[2])
        es = [jnp.exp(l - m) for l in ls]
        den = es[0] + es[1] + es[2]
        out_ref[...] = ((es[0] * o0[...] + es[1] * o1[...] + es[2] * o2[...]) / den).astype(out_ref.dtype)

    row = pl.BlockSpec((tm, W), lambda i: (i, 0))
    return _pcall(body, name=name, grid=(S // tm,), in_specs=[row] * 6, out_specs=row,
                  out_shape=jax.ShapeDtypeStruct((S, W), _CD), compiler_params=_params("parallel"))(*os_, *ls_)


def _dil_mix_bwd(doa, os_, ls_, *, name, tm=512):
    S, W = doa.shape

    def body(d_ref, o0, o1, o2, l0, l1, l2, do0, do1, do2, dl0, dl1, dl2):
        dv = d_ref[...]
        ls = [l0[...], l1[...], l2[...]]
        m = jnp.maximum(jnp.maximum(ls[0], ls[1]), ls[2])
        es = [jnp.exp(l - m) for l in ls]
        den = es[0] + es[1] + es[2]
        al = [e / den for e in es]
        da = [_head_sum(dv * o[...]) for o in (o0, o1, o2)]
        mean = al[0] * da[0] + al[1] * da[1] + al[2] * da[2]
        for a, d_, do_ref, dl_ref in zip(al, da, (do0, do1, do2), (dl0, dl1, dl2)):
            do_ref[...] = a * dv
            dl_ref[...] = a * (d_ - mean)

    row = pl.BlockSpec((tm, W), lambda i: (i, 0))
    shp = jax.ShapeDtypeStruct((S, W), F32)
    return _pcall(body, name=name, grid=(S // tm,), in_specs=[row] * 7, out_specs=[row] * 6, out_shape=[shp] * 6,
                  compiler_params=_params("parallel"))(doa, *os_, *ls_)


FOX_T = 512


PACK = 2 * HEAD_DIM
HEAD_PAIRS = N_FOX_HEADS // 2
Q_BLOCK0 = (3 * DIL_WIDTH) // PACK
K_BLOCK0 = (3 * DIL_WIDTH + FOX_WIDTH) // PACK
V_BLOCK0 = (3 * DIL_WIDTH + 2 * FOX_WIDTH) // PACK


def _pieces(x):
    hi = x.astype(jnp.bfloat16).astype(F32)
    r = x - hi
    mid = r.astype(jnp.bfloat16).astype(F32)
    lo = (r - mid).astype(jnp.bfloat16).astype(F32)
    return [hi, mid, lo]


def _extras(first, second, rows):
    lane = lax.broadcasted_iota(jnp.int32, (rows, HEAD_DIM), 1)
    out = jnp.zeros((rows, HEAD_DIM), F32)
    for idx, val in enumerate(list(first) + list(second)):
        out = jnp.where(lane == idx, val, out)
    return out


def _head_column(c, h):
    lane = lax.broadcasted_iota(jnp.int32, c.shape, 1)
    return jnp.sum(jnp.where(lane == h, c, 0.0), axis=1, keepdims=True)


ONES3 = [1.0, 1.0, 1.0]
ZEROS3 = [0.0, 0.0, 0.0]


def _fox_pack_fwd(qkv, c, *, name, tm=512):
    S = qkv.shape[0]

    def body(q_ref, k_ref, v_ref, c_ref, qo_ref, ko_ref, vo_ref):
        hp = pl.program_id(1)
        cv = c_ref[...]
        for hh in range(2):
            ch = _pieces(_head_column(cv, 2 * hp + hh))
            src = slice(hh * HEAD_DIM, (hh + 1) * HEAD_DIM)
            lo = slice(hh * PACK, hh * PACK + HEAD_DIM)
            hi = slice(hh * PACK + HEAD_DIM, (hh + 1) * PACK)
            qo_ref[:, lo] = (q_ref[:, src].astype(F32) * ATTN_SCALE).astype(qo_ref.dtype)
            qo_ref[:, hi] = _extras(ch, ONES3, tm).astype(qo_ref.dtype)
            ko_ref[:, lo] = k_ref[:, src]
            ko_ref[:, hi] = _extras(ONES3, [-p for p in ch], tm).astype(ko_ref.dtype)
            vo_ref[:, lo] = v_ref[:, src]
            vo_ref[:, hi] = _extras(ONES3, ZEROS3, tm).astype(vo_ref.dtype)

    def src(block0):
        return pl.BlockSpec((tm, PACK), lambda i, hp: (i, block0 + hp))

    out = pl.BlockSpec((tm, 2 * PACK), lambda i, hp: (i, hp))
    shp = jax.ShapeDtypeStruct((S, N_FOX_HEADS * PACK), _CD)
    return _pcall(body, name=name, grid=(S // tm, HEAD_PAIRS),
                  in_specs=[src(Q_BLOCK0), src(K_BLOCK0), src(V_BLOCK0), pl.BlockSpec((tm, PACK), lambda i, hp: (i, 0))],
                  out_specs=[out, out, out], out_shape=[shp, shp, shp],
                  compiler_params=_params("parallel", "parallel"))(qkv, qkv, qkv, c)


def _fox_fwd(qp, kp, vp, *, name):
    S = qp.shape[0]
    nt = S // FOX_T
    nt_dims = (((1,), (1,)), ((), ()))
    tn_dims = (((0,), (0,)), ((), ()))

    def body(q_ref, k_ref, v_ref, o_ref, l_ref, m_s, acc_s):
        i, j = pl.program_id(1), pl.program_id(2)

        @pl.when(j == 0)
        def _():
            m_s[...] = jnp.full((2, 1, FOX_T), NEG_INF, F32)
            acc_s[...] = jnp.zeros((2, PACK, FOX_T), F32)

        def tile(diagonal):
            for hh in range(2):
                cols = slice(hh * PACK, (hh + 1) * PACK)
                st = lax.dot_general(k_ref[:, cols], q_ref[:, cols], nt_dims, preferred_element_type=F32)
                if diagonal:
                    key = lax.broadcasted_iota(jnp.int32, (FOX_T, FOX_T), 0)
                    qry = lax.broadcasted_iota(jnp.int32, (FOX_T, FOX_T), 1)
                    st = jnp.where(key <= qry, st, NEG_INF)
                m_old = m_s[hh]
                m_new = jnp.maximum(m_old, jnp.max(st, axis=0, keepdims=True))
                pt = jnp.exp(st - m_new)
                acc_s[hh] = jnp.exp(m_old - m_new) * acc_s[hh] + lax.dot_general(
                    v_ref[:, cols], pt.astype(_CD), tn_dims, preferred_element_type=F32)
                m_s[hh] = m_new

        @pl.when(j < i)
        def _():
            tile(False)

        @pl.when(j == i)
        def _():
            tile(True)

        @pl.when(j == nt - 1)
        def _():
            for hh in range(2):
                acc = acc_s[hh]
                den = acc[HEAD_DIM:HEAD_DIM + 1, :]
                cols = slice(hh * HEAD_DIM, (hh + 1) * HEAD_DIM)
                o_ref[:, cols] = (acc[:HEAD_DIM, :] / den).T
                l_ref[:, cols] = jnp.broadcast_to(m_s[hh] + jnp.log(den), (HEAD_DIM, FOX_T)).T

    qs = pl.BlockSpec((FOX_T, 2 * PACK), lambda hp, i, j: (i, hp))
    ks = pl.BlockSpec((FOX_T, 2 * PACK), lambda hp, i, j: (jnp.minimum(i, j), hp))
    os_ = pl.BlockSpec((FOX_T, PACK), lambda hp, i, j: (i, hp))
    shp = jax.ShapeDtypeStruct((S, FOX_WIDTH), F32)
    return _pcall(body, name=name, grid=(HEAD_PAIRS, nt, nt), in_specs=[qs, ks, ks], out_specs=[os_, os_],
                  out_shape=[shp, shp],
                  scratch_shapes=[pltpu.VMEM((2, 1, FOX_T), F32), pltpu.VMEM((2, PACK, FOX_T), F32)],
                  compiler_params=_params("parallel", "parallel", "arbitrary"))(qp, kp, vp)


def _fox_pack_bwd(qkv, c, o, lse, do, *, name, tm=512):
    S = qkv.shape[0]

    def body(q_ref, c_ref, o_ref, l_ref, do_ref, qo_ref, do_out_ref):
        hp = pl.program_id(1)
        cv = c_ref[...]
        for hh in range(2):
            src = slice(hh * HEAD_DIM, (hh + 1) * HEAD_DIM)
            lo = slice(hh * PACK, hh * PACK + HEAD_DIM)
            hi = slice(hh * PACK + HEAD_DIM, (hh + 1) * PACK)
            shift = _head_column(cv, 2 * hp + hh) - l_ref[:, hh * HEAD_DIM:hh * HEAD_DIM + 1]
            dov = do_ref[:, src]
            dsum = jnp.sum(dov * o_ref[:, src], axis=-1, keepdims=True)
            qo_ref[:, lo] = (q_ref[:, src].astype(F32) * ATTN_SCALE).astype(qo_ref.dtype)
            qo_ref[:, hi] = _extras(_pieces(shift), ONES3, tm).astype(qo_ref.dtype)
            do_out_ref[:, lo] = dov.astype(do_out_ref.dtype)
            do_out_ref[:, hi] = _extras(_pieces(-dsum), ZEROS3, tm).astype(do_out_ref.dtype)

    pair = pl.BlockSpec((tm, PACK), lambda i, hp: (i, hp))
    out = pl.BlockSpec((tm, 2 * PACK), lambda i, hp: (i, hp))
    shp = jax.ShapeDtypeStruct((S, N_FOX_HEADS * PACK), _CD)
    return _pcall(body, name=name, grid=(S // tm, HEAD_PAIRS),
                  in_specs=[pl.BlockSpec((tm, PACK), lambda i, hp: (i, Q_BLOCK0 + hp)),
                            pl.BlockSpec((tm, PACK), lambda i, hp: (i, 0)), pair, pair, pair],
                  out_specs=[out, out], out_shape=[shp, shp],
                  compiler_params=_params("parallel", "parallel"))(qkv, c, o, lse, do)


def _fox_bwd(qp, kp, vp, dop, *, name):
    S = qp.shape[0]
    nt = S // FOX_T
    nt_dims = (((1,), (1,)), ((), ()))
    tn_dims = (((0,), (0,)), ((), ()))

    def body(q_ref, k_ref, v_ref, do_ref, dq_ref, dk_ref, dv_ref, dc_ref, dr_ref, dq_s, dk_s, dv_s, dc_s, dr_s):
        j, i = pl.program_id(1), pl.program_id(2)

        @pl.when(jnp.logical_and(j == 0, i == 0))
        def _():
            dq_s[...] = jnp.zeros((S, 2 * PACK), F32)
            dr_s[...] = jnp.zeros((2, 1, S), F32)

        @pl.when(i == 0)
        def _():
            dk_s[...] = jnp.zeros((FOX_T, 2 * PACK), F32)
            dv_s[...] = jnp.zeros((FOX_T, 2 * PACK), F32)
            dc_s[...] = jnp.zeros((2, FOX_T, 1), F32)

        def tile(diagonal):
            rows = pl.ds(pl.multiple_of(i * FOX_T, FOX_T), FOX_T)
            for hh in range(2):
                cols = slice(hh * PACK, (hh + 1) * PACK)
                qv, kv, vv, dov = q_ref[:, cols], k_ref[:, cols], v_ref[:, cols], do_ref[:, cols]
                pt = jnp.exp(lax.dot_general(kv, qv, nt_dims, preferred_element_type=F32))
                if diagonal:
                    key = lax.broadcasted_iota(jnp.int32, (FOX_T, FOX_T), 0)
                    qry = lax.broadcasted_iota(jnp.int32, (FOX_T, FOX_T), 1)
                    pt = jnp.where(key <= qry, pt, 0.0)
                dst = pt * lax.dot_general(vv, dov, nt_dims, preferred_element_type=F32)
                dsb = dst.astype(_CD)
                dc_s[hh] += jnp.sum(dst, axis=1, keepdims=True)
                dr_s[hh, :, rows] += jnp.sum(dst, axis=0, keepdims=True)
                dv_s[:, cols] += jnp.dot(pt.astype(_CD), dov, preferred_element_type=F32)
                dk_s[:, cols] += jnp.dot(dsb, qv, preferred_element_type=F32)
                dq_s[rows, cols] += lax.dot_general(dsb, kv, tn_dims, preferred_element_type=F32)

        @pl.when(i > j)
        def _():
            tile(False)

        @pl.when(i == j)
        def _():
            tile(True)

        @pl.when(i == nt - 1)
        def _():
            dk_ref[...] = dk_s[...].astype(dk_ref.dtype)
            dv_ref[...] = dv_s[...].astype(dv_ref.dtype)
            for hh in range(2):
                dc_ref[:, hh * HEAD_DIM:(hh + 1) * HEAD_DIM] = jnp.broadcast_to(dc_s[hh], (FOX_T, HEAD_DIM))

        @pl.when(jnp.logical_and(j == nt - 1, i == nt - 1))
        def _():
            lane = lax.broadcasted_iota(jnp.int32, (S, 2 * PACK), 1) % PACK
            dq_ref[...] = (dq_s[...] * jnp.where(lane < HEAD_DIM, ATTN_SCALE, 1.0)).astype(dq_ref.dtype)
            dr_ref[...] = dr_s[...]

    qs = pl.BlockSpec((FOX_T, 2 * PACK), lambda hp, j, i: (jnp.maximum(i, j), hp))
    ks = pl.BlockSpec((FOX_T, 2 * PACK), lambda hp, j, i: (j, hp))
    whole = pl.BlockSpec((S, 2 * PACK), lambda hp, j, i: (0, hp))
    cs = pl.BlockSpec((FOX_T, PACK), lambda hp, j, i: (j, hp))
    rs = pl.BlockSpec((2, 1, S), lambda hp, j, i: (hp, 0, 0))
    shp = jax.ShapeDtypeStruct((S, N_FOX_HEADS * PACK), _CD)
    return _pcall(body, name=name, grid=(HEAD_PAIRS, nt, nt), in_specs=[qs, ks, ks, qs],
                  out_specs=[whole, ks, ks, cs, rs],
                  out_shape=[shp, shp, shp, jax.ShapeDtypeStruct((S, FOX_WIDTH), F32),
                             jax.ShapeDtypeStruct((N_FOX_HEADS, 1, S), F32)],
                  scratch_shapes=[pltpu.VMEM((S, 2 * PACK), F32), pltpu.VMEM((FOX_T, 2 * PACK), F32),
                                  pltpu.VMEM((FOX_T, 2 * PACK), F32), pltpu.VMEM((2, FOX_T, 1), F32),
                                  pltpu.VMEM((2, 1, S), F32)],
                  compiler_params=_params("parallel", "arbitrary", "arbitrary"))(qp, kp, vp, dop)


def _dedilate(t, d):
    if d == 1:
        return t
    S, C = t.shape
    return t.reshape(S // d, d, C).transpose(1, 0, 2).reshape(S, C)


def _redilate(t, d):
    if d == 1:
        return t
    S, C = t.shape
    return t.reshape(d, S // d, C).transpose(1, 0, 2).reshape(S, C)


def _layer_step(x, tgt, w, p):
    S = x.shape[0]
    h = _rms_fwd(x, p["norm_mix_g"], name="rms_mix")
    qkv = _mm(h, w["qkv"], name="proj_qkv", out_dtype=_CD, tn=768)
    zf = _mm(h, w["f"], name="proj_f")
    gl = _mm(h, w["g"], name="proj_gate", tn=1024)

    dil_q, dil_k, dil_v = [], [], []
    dil_o, dil_l = [], []
    for g, (_, d) in enumerate(DIL_PAIRS):
        qg = _dedilate(qkv[:, g * DIL_OUT:(g + 1) * DIL_OUT], d)
        kg = _dedilate(qkv[:, DIL_WIDTH + g * DIL_OUT:DIL_WIDTH + (g + 1) * DIL_OUT], d)
        vg = _dedilate(qkv[:, 2 * DIL_WIDTH + g * DIL_OUT:2 * DIL_WIDTH + (g + 1) * DIL_OUT], d)
        og, lg = _dil_fwd(qg, kg, vg, g, name=f"dil_fwd{g}")
        dil_q.append(qg), dil_k.append(kg), dil_v.append(vg)
        dil_o.append(_redilate(og, d)), dil_l.append(_redilate(lg, d))
    o_a = _dil_mix_fwd(dil_o, dil_l, name="dil_mix")

    c = _fox_cumsum(zf, p["b_fgt"], name="fox_cumsum")
    fqp, fkp, fvp = _fox_pack_fwd(qkv, c, name="fox_pack")
    o_b, flse = _fox_fwd(fqp, fkp, fvp, name="fox_fwd")

    y_a = _mm(o_a, w["dil_out"], name="y_a", tn=1024)
    y_b = _mm(o_b, w["fox_out"], name="y_b", tn=1024)
    merged = _gate_fwd(gl, p["b_gate"], y_a, y_b, name="gate_fwd")
    x1 = _mm(merged, w["out"], name="mix_out", add=x)

    h2 = _rms_fwd(x1, p["norm_ffn_g"], name="rms_ffn")
    gu = _mm(h2, w["ffn_in"], name="ffn_in", tn=1408)
    act = _swiglu_fwd(gu, name="swiglu")
    x2 = _mm(act, w["ffn_down"], name="ffn_down", add=x1, tk=2816)

    loss, dx2, dg_final = _loss_head(x2, p["norm_final_g"], tgt, name="loss_head")

    dact = _mm(dx2, w["ffn_down"], name="d_act", tb=True, tn=1408)
    gw_ffn_down = _mm(act, dx2, name="gw_ffn_down", ta=True, out_dtype=_CD, tm=1408)
    dgu = _swiglu_bwd(dact, gu, name="swiglu_bwd")
    dh2 = _mm(dgu, w["ffn_in"], name="d_h2", tb=True, tk=2816)
    gw_ffn_in = _mm(h2, dgu, name="gw_ffn_in", ta=True, out_dtype=_CD)
    dx1, dg_ffn = _rms_bwd(x1, p["norm_ffn_g"], dh2, dx2, name="rms_ffn_bwd")

    dmerged = _mm(dx1, w["out"], name="d_merged", tb=True)
    gw_out = _mm(merged, dx1, name="gw_out", ta=True, out_dtype=_CD)
    dy_a, dy_b, dgl, db_gate = _gate_bwd(dmerged, gl, p["b_gate"], y_a, y_b, name="gate_bwd")
    do_a = _mm(dy_a, w["dil_out"], name="d_o_a", tb=True)
    gw_dil_out = _mm(o_a, dy_a, name="gw_dil_out", ta=True, out_dtype=_CD, tn=1024)
    do_b = _mm(dy_b, w["fox_out"], name="d_o_b", tb=True)
    gw_fox_out = _mm(o_b, dy_b, name="gw_fox_out", ta=True, out_dtype=_CD, tn=1024)

    bqp, bdop = _fox_pack_bwd(qkv, c, o_b, flse, do_b, name="fox_pack_bwd")
    dqp, dkp, dvp, dck, dcq = _fox_bwd(bqp, fkp, fvp, bdop, name="fox_bwd")
    dc = dcq[:, 0, :].T - dck.reshape(S, N_FOX_HEADS, HEAD_DIM)[:, :, 0]
    dc = jnp.pad(dc, ((0, 0), (0, F_PAD - N_FOX_HEADS)))
    dzf, db_fgt = _fox_cumsum_bwd(dc, zf, p["b_fgt"], name="fox_cumsum_bwd")

    def unpack(t):
        return t.reshape(S, N_FOX_HEADS, PACK)[:, :, :HEAD_DIM].reshape(S, FOX_WIDTH)

    douts = _dil_mix_bwd(do_a, dil_o, dil_l, name="dil_mix_bwd")
    dqs, dks, dvs = [], [], []
    for g, (_, d) in enumerate(DIL_PAIRS):
        dq, dk, dv = _dil_bwd(dil_q[g], dil_k[g], dil_v[g], _dedilate(dil_o[g], d), _dedilate(dil_l[g], d),
                              _dedilate(douts[g], d), _dedilate(douts[3 + g], d), g, name=f"dil_bwd{g}")
        dqs.append(_redilate(dq, d)), dks.append(_redilate(dk, d)), dvs.append(_redilate(dv, d))
    dqkv = jnp.concatenate(dqs + dks + dvs + [unpack(dqp), unpack(dkp), unpack(dvp)], axis=1)

    dh = _mm(dqkv, w["qkv"], name="d_h_qkv", tb=True, tk=1920)
    dh = _mm(dgl, w["g"], name="d_h_gate", tb=True, add=dh)
    dh = _mm(dzf, w["f"], name="d_h_f", tb=True, add=dh)
    gw_qkv = _mm(h, dqkv, name="gw_qkv", ta=True, out_dtype=_CD, tn=768)
    gw_g = _mm(h, dgl, name="gw_gate", ta=True, out_dtype=_CD)
    gw_f = _mm(h, dzf, name="gw_f", ta=True, out_dtype=_CD)
    dx, dg_mix = _rms_bwd(x, p["norm_mix_g"], dh, dx1, name="rms_mix_bwd")

    gw = dict(qkv=gw_qkv, f=gw_f, g=gw_g, dil_out=gw_dil_out, fox_out=gw_fox_out, out=gw_out, ffn_in=gw_ffn_in,
              ffn_down=gw_ffn_down)
    small = dict(norm_mix_g=dg_mix, b_fgt=db_fgt, b_gate=db_gate, norm_ffn_g=dg_ffn, norm_final_g=dg_final)
    return loss, dx, gw, small


def _position():
    return lax.axis_index("x"), lax.axis_index("y"), lax.axis_index("c")


def _other_chips(x, y):
    return [(1 - x, y), (x, 1 - y), (1 - x, 1 - y)]


def _gather_weights(shards):
    n = len(shards)

    def body(*refs):
        ins, outs = refs[:n], refs[n:2 * n]
        send_sems, recv_sems = refs[2 * n:]
        x, y, c = _position()
        mine = 2 * x + y
        chips = _other_chips(x, y)
        started = []
        for w in range(n):
            half = ins[w].shape[0] // 2
            rows = pl.ds(c * half, half)
            for r, (cx, cy) in enumerate(chips):
                cp = pltpu.make_async_remote_copy(
                    src_ref=ins[w].at[rows, :], dst_ref=outs[w].at[mine, rows, :], send_sem=send_sems.at[w, r],
                    recv_sem=recv_sems.at[w, r], device_id=(cx, cy, c), device_id_type=MESH)
                cp.start()
                started.append(cp)

        def landed(w, r, rows, peer):
            cx, cy = chips[r % 3]
            blk = outs[w].at[2 * cx + cy, rows, :]
            return pltpu.make_async_remote_copy(src_ref=blk, dst_ref=blk, send_sem=send_sems.at[w, r],
                                                recv_sem=recv_sems.at[w, r], device_id=peer, device_id_type=MESH)

        for w in range(n):
            half = ins[w].shape[0] // 2
            rows = pl.ds(c * half, half)
            for r, (cx, cy) in enumerate(chips):
                landed(w, r, rows, (cx, cy, c)).wait_recv()
                fwd = landed(w, 3 + r, rows, (x, y, 1 - c))
                fwd.start()
                started.append(fwd)
        for w in range(n):
            half = ins[w].shape[0] // 2
            rows = pl.ds((1 - c) * half, half)
            for r in range(3):
                landed(w, 3 + r, rows, (x, y, 1 - c)).wait_recv()
        for cp in started:
            cp.wait_send()

    return _pcall(
        body, name="gather_weights", in_specs=[HBM_SPEC] * n, out_specs=[HBM_SPEC] * n,
        out_shape=[jax.ShapeDtypeStruct((4,) + s.shape, s.dtype) for s in shards],
        scratch_shapes=[pltpu.SemaphoreType.DMA((n, 6)), pltpu.SemaphoreType.DMA((n, 6))],
    )(*shards)


def _swap_halves(grads):
    n = len(grads)

    def body(*refs):
        ins, outs = refs[:n], refs[n:2 * n]
        send_sems, recv_sems = refs[2 * n:]
        x, y, c = _position()
        copies = []
        for w in range(n):
            half = ins[w].shape[1] // 2
            cp = pltpu.make_async_remote_copy(
                src_ref=ins[w].at[:, pl.ds((1 - c) * half, half), :], dst_ref=outs[w], send_sem=send_sems.at[w],
                recv_sem=recv_sems.at[w], device_id=(x, y, 1 - c), device_id_type=MESH)
            cp.start()
            copies.append(cp)
        for cp in copies:
            cp.wait()

    return _pcall(
        body, name="swap_halves", in_specs=[HBM_SPEC] * n, out_specs=[HBM_SPEC] * n,
        out_shape=[jax.ShapeDtypeStruct((4, g.shape[1] // 2, g.shape[2]), g.dtype) for g in grads],
        scratch_shapes=[pltpu.SemaphoreType.DMA((n,)), pltpu.SemaphoreType.DMA((n,))],
    )(*grads)


def _scatter_to_owners(parts):
    n = len(parts)

    def body(*refs):
        ins, outs = refs[:n], refs[n:2 * n]
        send_sems, recv_sems = refs[2 * n:]
        x, y, c = _position()
        copies = []
        for w in range(n):
            for r, (cx, cy) in enumerate(_other_chips(x, y)):
                cp = pltpu.make_async_remote_copy(
                    src_ref=ins[w].at[2 * cx + cy], dst_ref=outs[w].at[r], send_sem=send_sems.at[w, r],
                    recv_sem=recv_sems.at[w, r], device_id=(cx, cy, c), device_id_type=MESH)
                cp.start()
                copies.append(cp)
        for cp in copies:
            cp.wait()

    return _pcall(
        body, name="scatter_to_owners", in_specs=[HBM_SPEC] * n, out_specs=[HBM_SPEC] * n,
        out_shape=[jax.ShapeDtypeStruct((3,) + p.shape[1:], p.dtype) for p in parts],
        scratch_shapes=[pltpu.SemaphoreType.DMA((n, 3)), pltpu.SemaphoreType.DMA((n, 3))],
    )(*parts)


def _share_halves(halves):
    n = len(halves)

    def body(*refs):
        ins, outs = refs[:n], refs[n:2 * n]
        send_sems, recv_sems = refs[2 * n:]
        x, y, c = _position()
        copies = []
        for w in range(n):
            cp = pltpu.make_async_remote_copy(src_ref=ins[w], dst_ref=outs[w], send_sem=send_sems.at[w],
                                              recv_sem=recv_sems.at[w], device_id=(x, y, 1 - c), device_id_type=MESH)
            cp.start()
            copies.append(cp)
        for cp in copies:
            cp.wait()

    return _pcall(
        body, name="share_halves", in_specs=[HBM_SPEC] * n, out_specs=[HBM_SPEC] * n,
        out_shape=[jax.ShapeDtypeStruct(h.shape, h.dtype) for h in halves],
        scratch_shapes=[pltpu.SemaphoreType.DMA((n,)), pltpu.SemaphoreType.DMA((n,))],
    )(*halves)


def _sum_small(part):
    rows, width = part.shape

    def body(x_ref, out_ref, all_ref, send_sems, recv_sems):
        x, y, c = _position()
        me, sibling = (x, y, c), (x, y, 1 - c)
        chips = _other_chips(x, y)

        def block(px, py, pc):
            return all_ref.at[pl.ds((4 * px + 2 * py + pc) * rows, rows), :]

        def copy(k, blk, to, src=None):
            return pltpu.make_async_remote_copy(
                src_ref=block(*blk) if src is None else src, dst_ref=block(*blk), send_sem=send_sems.at[k],
                recv_sem=recv_sems.at[k], device_id=to, device_id_type=MESH)

        all_ref[pl.ds((4 * x + 2 * y + c) * rows, rows), :] = x_ref[...]
        first = [copy(0, me, sibling, src=x_ref)]
        first += [copy(1 + j, me, (*chip, c), src=x_ref) for j, chip in enumerate(chips)]
        for cp in first:
            cp.start()
        passed = [copy(4 + j, (*chip, c), sibling) for j, chip in enumerate(chips)]
        for j, chip in enumerate(chips):
            copy(1 + j, (*chip, c), me).wait_recv()
            passed[j].start()
        copy(0, sibling, me).wait_recv()
        for j, chip in enumerate(chips):
            copy(4 + j, (*chip, 1 - c), me).wait_recv()
        for cp in first + passed:
            cp.wait_send()
        total = all_ref[0:rows, :]
        for d in range(1, 8):
            total = total + all_ref[d * rows:(d + 1) * rows, :]
        out_ref[...] = total

    vm = pl.BlockSpec(memory_space=pltpu.VMEM)
    return _pcall(
        body, name="sum_small", in_specs=[vm], out_specs=vm, out_shape=jax.ShapeDtypeStruct((rows, width), F32),
        scratch_shapes=[pltpu.VMEM((8 * rows, width), F32), pltpu.SemaphoreType.DMA((7,)), pltpu.SemaphoreType.DMA((7,))],
    )(part)


def _row_tile(R, C, itemsize=4, budget=1 << 20):
    for t in (512, 256, 128, 64, 32, 16, 8):
        if R % t == 0 and t * C * itemsize <= budget:
            return t
    return R


def _add_halves(g, recv, c, *, name):
    _, R, C = g.shape
    half = R // 2
    t = _row_tile(half, C)
    nb = half // t

    def body(c_ref, g_ref, r_ref, o_ref):
        o_ref[...] = (g_ref[...].astype(F32) + r_ref[...].astype(F32)).astype(o_ref.dtype)

    grid_spec = pltpu.PrefetchScalarGridSpec(
        num_scalar_prefetch=1, grid=(4, nb),
        in_specs=[pl.BlockSpec((1, t, C), lambda k, i, cr: (k, cr[0] * nb + i, 0)),
                  pl.BlockSpec((1, t, C), lambda k, i, cr: (k, i, 0))],
        out_specs=pl.BlockSpec((1, t, C), lambda k, i, cr: (k, i, 0)))
    return _pcall(body, name=name, grid_spec=grid_spec, out_shape=jax.ShapeDtypeStruct((4, half, C), g.dtype),
                  compiler_params=_params("parallel", "parallel"))(c, g, recv)


def _add_owners(mine, recv, *, name):
    half, C = mine.shape
    t = _row_tile(half, C)

    def body(m_ref, r_ref, o_ref):
        o_ref[...] = ((m_ref[...].astype(F32) + r_ref[0].astype(F32)) + r_ref[1].astype(F32)) + r_ref[2].astype(F32)

    return _pcall(body, name=name, grid=(half // t,),
                  in_specs=[pl.BlockSpec((t, C), lambda i: (i, 0)), pl.BlockSpec((3, t, C), lambda i: (0, i, 0))],
                  out_specs=pl.BlockSpec((t, C), lambda i: (i, 0)), out_shape=jax.ShapeDtypeStruct((half, C), F32),
                  compiler_params=_params("parallel"))(mine, recv)


def _adamw(w, g, m, v, *, name):
    R, C = w.shape
    t = _row_tile(R, C)
    c1 = 1.0 - ADAM_B1 ** ADAM_STEP
    c2 = 1.0 - ADAM_B2 ** ADAM_STEP

    def body(w_ref, g_ref, m_ref, v_ref, d_ref, nm_ref, nv_ref):
        gv = g_ref[...]
        mn = ADAM_B1 * m_ref[...] + (1.0 - ADAM_B1) * gv
        vn = ADAM_B2 * v_ref[...] + (1.0 - ADAM_B2) * (gv * gv)
        d_ref[...] = -ADAM_LR * ((mn / c1) / (jnp.sqrt(vn / c2) + ADAM_EPS) + ADAM_WD * w_ref[...])
        nm_ref[...] = mn
        nv_ref[...] = vn

    blk = pl.BlockSpec((t, C), lambda i: (i, 0))
    shp = jax.ShapeDtypeStruct((R, C), F32)
    return _pcall(body, name=name, grid=(R // t,), in_specs=[blk] * 4, out_specs=[blk] * 3, out_shape=[shp] * 3,
                  compiler_params=_params("parallel"))(w, g, m, v)


BIG = ("w_in", "w_dil_out", "w_fox_out", "w_out", "w_ffn_in", "w_ffn_down")
SMALL = ("norm_mix_g", "b_fgt", "b_gate", "norm_ffn_g", "norm_final_g")
ORDER = ("norm_mix_g", "w_in", "b_fgt", "b_gate", "w_dil_out", "w_fox_out", "w_out", "norm_ffn_g", "w_ffn_in",
         "w_ffn_down", "norm_final_g")
SMALL_ROWS = {"norm_mix_g": (0, 1), "b_gate": (1, 3), "norm_ffn_g": (3, 4), "norm_final_g": (4, 5), "b_fgt": (5, 6)}


def _columns_to_blocks(full, ncol):
    K = full.shape[0]
    return full.reshape(K, 4, ncol).transpose(1, 0, 2)


def _blocks_to_columns(blocks):
    n, K, ncol = blocks.shape
    return blocks.transpose(1, 0, 2).reshape(K, n * ncol)


def kernel(x, norm_mix_g, w_in, b_fgt, b_gate, w_dil_out, w_fox_out, w_out, norm_ffn_g, w_ffn_in, w_ffn_down, norm_final_g, loss_target, m_norm_mix_g, m_w_in, m_b_fgt, m_b_gate, m_w_dil_out, m_w_fox_out, m_w_out, m_norm_ffn_g, m_w_ffn_in, m_w_ffn_down, m_norm_final_g, v_norm_mix_g, v_w_in, v_b_fgt, v_b_gate, v_w_dil_out, v_w_fox_out, v_w_out, v_norm_ffn_g, v_w_ffn_in, v_w_ffn_down, v_norm_final_g):
    weights = dict(norm_mix_g=norm_mix_g, w_in=w_in, b_fgt=b_fgt, b_gate=b_gate, w_dil_out=w_dil_out,
                   w_fox_out=w_fox_out, w_out=w_out, norm_ffn_g=norm_ffn_g, w_ffn_in=w_ffn_in, w_ffn_down=w_ffn_down,
                   norm_final_g=norm_final_g)
    m_in = dict(norm_mix_g=m_norm_mix_g, w_in=m_w_in, b_fgt=m_b_fgt, b_gate=m_b_gate, w_dil_out=m_w_dil_out,
                w_fox_out=m_w_fox_out, w_out=m_w_out, norm_ffn_g=m_norm_ffn_g, w_ffn_in=m_w_ffn_in,
                w_ffn_down=m_w_ffn_down, norm_final_g=m_norm_final_g)
    v_in = dict(norm_mix_g=v_norm_mix_g, w_in=v_w_in, b_fgt=v_b_fgt, b_gate=v_b_gate, w_dil_out=v_w_dil_out,
                w_fox_out=v_w_fox_out, w_out=v_w_out, norm_ffn_g=v_norm_ffn_g, w_ffn_in=v_w_ffn_in,
                w_ffn_down=v_w_ffn_down, norm_final_g=v_norm_final_g)
    c = lax.axis_index("c")
    chip = 2 * lax.axis_index("x") + lax.axis_index("y")

    shards = [weights[n][0].astype(_CD) for n in BIG]
    gathered = _gather_weights(shards)
    g_in, g_dil, g_fox, g_out, g_ffn_in, g_ffn_down = [
        lax.dynamic_update_index_in_dim(g, s, chip, 0) for g, s in zip(gathered, shards)]
    full_in = _blocks_to_columns(g_in)
    o3 = QKV_COLS
    o4 = o3 + N_FOX_HEADS
    w = dict(
        qkv=full_in[:, :o3],
        f=jnp.pad(full_in[:, o3:o4], ((0, 0), (0, F_PAD - N_FOX_HEADS))),
        g=full_in[:, o4:],
        dil_out=_blocks_to_columns(g_dil), fox_out=_blocks_to_columns(g_fox),
        out=g_out.reshape(D_MODEL, D_MODEL), ffn_in=_blocks_to_columns(g_ffn_in),
        ffn_down=g_ffn_down.reshape(D_FF, D_MODEL))
    p = dict(norm_mix_g=norm_mix_g, b_fgt=jnp.pad(b_fgt, ((0, 0), (0, F_PAD - N_FOX_HEADS))), b_gate=b_gate,
             norm_ffn_g=norm_ffn_g, norm_final_g=norm_final_g.reshape(1, D_MODEL))

    loss_part, grad_x, gw, small = _layer_step(x[0], loss_target[0], w, p)

    gw_in = jnp.concatenate([gw["qkv"], gw["f"][:, :N_FOX_HEADS], gw["g"]], axis=1)
    blocks = [
        _columns_to_blocks(gw_in, w_in.shape[2]),
        _columns_to_blocks(gw["dil_out"], w_dil_out.shape[2]),
        _columns_to_blocks(gw["fox_out"], w_fox_out.shape[2]),
        gw["out"].reshape(4, w_out.shape[1], D_MODEL),
        _columns_to_blocks(gw["ffn_in"], w_ffn_in.shape[2]),
        gw["ffn_down"].reshape(4, w_ffn_down.shape[1], D_MODEL),
    ]
    from_sibling = _swap_halves(blocks)
    c_arr = jnp.reshape(c, (1,)).astype(jnp.int32)
    chip_sums = [_add_halves(b, r, c_arr, name=f"add_halves_{n}") for b, r, n in zip(blocks, from_sibling, BIG)]
    from_chips = _scatter_to_owners(chip_sums)
    halves = [_add_owners(lax.dynamic_index_in_dim(s, chip, 0, keepdims=False), r, name=f"add_owners_{n}")
              for s, r, n in zip(chip_sums, from_chips, BIG)]
    grads = {}
    for n, own, other in zip(BIG, halves, _share_halves(halves)):
        pair = jnp.stack([own, other])
        grads[n] = jnp.where(c == 0, pair, pair[::-1]).reshape(2 * own.shape[0], own.shape[1])

    packed = jnp.concatenate([
        small["norm_mix_g"], small["b_gate"].reshape(2, D_MODEL), small["norm_ffn_g"], small["norm_final_g"],
        jnp.pad(small["b_fgt"], ((0, 0), (0, D_MODEL - F_PAD))), jnp.zeros((2, D_MODEL), F32)], axis=0)
    summed = _sum_small(packed)
    for n in SMALL:
        lo, hi = SMALL_ROWS[n]
        grads[n] = summed[lo:hi].reshape(1, -1)[:, :weights[n].size]

    loss = lax.psum(loss_part[0, 0], ("x", "y", "c"))

    out_g, out_d, out_m, out_v = {}, {}, {}, {}
    for n in ORDER:
        shape = weights[n].shape
        two_d = shape[1:] if len(shape) == 3 else (1, weights[n].size)
        g2 = grads[n].reshape(two_d)
        d2, m2, v2 = _adamw(weights[n].reshape(two_d), g2, m_in[n].reshape(two_d), v_in[n].reshape(two_d),
                            name=f"adamw_{n}")
        out_g[n], out_d[n], out_m[n], out_v[n] = (g2.reshape(shape), d2.reshape(shape), m2.reshape(shape),
                                                  v2.reshape(shape))
    return (loss, grad_x[None], *[out_g[n] for n in ORDER], *[out_d[n] for n in ORDER],
            *[out_m[n] for n in ORDER], *[out_v[n] for n in ORDER])
```

```python
import numpy as np
import jax
import jax.numpy as jnp
from jax import lax
from jax.experimental import pallas as pl
from jax.experimental.pallas import tpu as pltpu

F32 = jnp.float32
_CD = jnp.bfloat16

D_MODEL = 1024
HEAD_DIM = 64
DIL_PAIRS = ((128, 1), (512, 4), (2048, 16))
N_DIL_GROUPS = 3
DIL_HEADS = 4
DIL_W = 128
DIL_OUT = DIL_HEADS * HEAD_DIM
DIL_WIDTH = N_DIL_GROUPS * DIL_OUT
N_FOX_HEADS = 8
FOX_WIDTH = N_FOX_HEADS * HEAD_DIM
D_FF = 2816
QKV_COLS = 3 * DIL_WIDTH + 3 * FOX_WIDTH
F_PAD = 128
RMS_EPS = 1e-6
NEG_INF = -1e30
ATTN_SCALE = HEAD_DIM ** -0.5
ADAM_LR, ADAM_B1, ADAM_B2, ADAM_EPS, ADAM_WD, ADAM_STEP = 0.001, 0.9, 0.999, 1e-08, 0.01, 10

VMEM_LIMIT = 48 * 1024 * 1024
MESH = pl.DeviceIdType.MESH
HBM_SPEC = pl.BlockSpec(memory_space=pltpu.HBM)


def _pcall(body, **kw):
    return pl.pallas_call(body, **kw)


def _params(*sem):
    return pltpu.CompilerParams(dimension_semantics=sem, vmem_limit_bytes=VMEM_LIMIT)


def _pick(dim, pref):
    t = (min(pref, dim) // 128) * 128
    while t >= 128:
        if dim % t == 0:
            return t
        t -= 128
    return dim


def _mm(a, b, *, name, ta=False, tb=False, out_dtype=F32, add=None, tm=1024, tn=512, tk=2048):
    if ta:
        K, M = a.shape
    else:
        M, K = a.shape
    if tb:
        N, K2 = b.shape
    else:
        K2, N = b.shape
    assert K == K2, (a.shape, b.shape)
    tm, tn, tk = _pick(M, tm), _pick(N, tn), _pick(K, tk)
    nk = K // tk
    dn = (((0 if ta else 1,), (1 if tb else 0,)), ((), ()))
    has_add = add is not None

    def body(*refs):
        a_ref, b_ref = refs[0], refs[1]
        add_ref = refs[2] if has_add else None
        o_ref = refs[3] if has_add else refs[2]
        p = lax.dot_general(a_ref[...].astype(_CD), b_ref[...].astype(_CD), dn, preferred_element_type=F32)

        def finish(r):
            if has_add:
                r = r + add_ref[...]
            o_ref[...] = r.astype(out_dtype)

        if nk == 1:
            finish(p)
        else:
            acc_ref = refs[-1]
            k = pl.program_id(2)

            @pl.when(k == 0)
            def _():
                acc_ref[...] = p

            @pl.when(k > 0)
            def _():
                acc_ref[...] += p

            @pl.when(k == nk - 1)
            def _():
                finish(acc_ref[...])

    a_spec = pl.BlockSpec((tk, tm), lambda i, j, k: (k, i)) if ta else pl.BlockSpec((tm, tk), lambda i, j, k: (i, k))
    b_spec = pl.BlockSpec((tn, tk), lambda i, j, k: (j, k)) if tb else pl.BlockSpec((tk, tn), lambda i, j, k: (k, j))
    o_spec = pl.BlockSpec((tm, tn), lambda i, j, k: (i, j))
    in_specs = [a_spec, b_spec] + ([o_spec] if has_add else [])
    args = (a, b) + ((add,) if has_add else ())
    return _pcall(
        body, name=name, grid=(M // tm, N // tn, nk), in_specs=in_specs, out_specs=o_spec,
        out_shape=jax.ShapeDtypeStruct((M, N), out_dtype),
        scratch_shapes=[pltpu.VMEM((tm, tn), F32)] if nk > 1 else [],
        compiler_params=_params("parallel", "parallel", "arbitrary"),
    )(*args)


def _rms_fwd(x, g, *, name, tm=512, after=None):
    S, D = x.shape

    def body(x_ref, g_ref, *rest):
        h_ref = rest[-1]
        xv = x_ref[...]
        r = lax.rsqrt(jnp.mean(xv * xv, axis=-1, keepdims=True) + RMS_EPS)
        h_ref[...] = ((xv * r) * g_ref[...]).astype(h_ref.dtype)

    row = pl.BlockSpec((tm, D), lambda i: (i, 0))
    extra = [] if after is None else [after]
    return _pcall(body, name=name, grid=(S // tm,),
                  in_specs=[row, pl.BlockSpec((1, D), lambda i: (0, 0))] + [pl.BlockSpec(memory_space=pl.ANY)] * len(extra),
                  out_specs=row, out_shape=jax.ShapeDtypeStruct((S, D), _CD),
                  compiler_params=_params("parallel"))(x, g, *extra)


def _rms_bwd(x, g, dh, dres, *, name, tm=512):
    S, D = x.shape

    def body(x_ref, g_ref, dh_ref, dres_ref, dx_ref, dg_ref):
        xv = x_ref[...]
        r = lax.rsqrt(jnp.mean(xv * xv, axis=-1, keepdims=True) + RMS_EPS)
        xh = xv * r
        dhv = dh_ref[...]
        dxh = dhv * g_ref[...]
        dx_ref[...] = dres_ref[...] + r * (dxh - xh * jnp.mean(dxh * xh, axis=-1, keepdims=True))
        part = jnp.sum(dhv * xh, axis=0, keepdims=True)

        @pl.when(pl.program_id(0) == 0)
        def _():
            dg_ref[...] = part

        @pl.when(pl.program_id(0) > 0)
        def _():
            dg_ref[...] += part

    row = pl.BlockSpec((tm, D), lambda i: (i, 0))
    vec = pl.BlockSpec((1, D), lambda i: (0, 0))
    return _pcall(body, name=name, grid=(S // tm,), in_specs=[row, vec, row, row], out_specs=[row, vec],
                  out_shape=[jax.ShapeDtypeStruct((S, D), F32), jax.ShapeDtypeStruct((1, D), F32)],
                  compiler_params=_params("arbitrary"))(x, g, dh, dres)


def _loss_head(x, g, tgt, *, name, tm=512):
    S, D = x.shape

    def body(x_ref, g_ref, t_ref, loss_ref, dx_ref, dg_ref):
        xv = x_ref[...]
        gv = g_ref[...]
        r = lax.rsqrt(jnp.mean(xv * xv, axis=-1, keepdims=True) + RMS_EPS)
        xh = xv * r
        err = xh * gv - t_ref[...]
        lpart = 0.5 * jnp.sum(jnp.mean(err * err, axis=-1, keepdims=True), axis=0, keepdims=True)
        dy = err * (1.0 / D)
        dxh = dy * gv
        dx_ref[...] = r * (dxh - xh * jnp.mean(dxh * xh, axis=-1, keepdims=True))
        gpart = jnp.sum(dy * xh, axis=0, keepdims=True)

        @pl.when(pl.program_id(0) == 0)
        def _():
            loss_ref[...] = lpart
            dg_ref[...] = gpart

        @pl.when(pl.program_id(0) > 0)
        def _():
            loss_ref[...] += lpart
            dg_ref[...] += gpart

    row = pl.BlockSpec((tm, D), lambda i: (i, 0))
    vec = pl.BlockSpec((1, D), lambda i: (0, 0))
    one = pl.BlockSpec((1, 1), lambda i: (0, 0))
    return _pcall(body, name=name, grid=(S // tm,), in_specs=[row, vec, row], out_specs=[one, row, vec],
                  out_shape=[jax.ShapeDtypeStruct((1, 1), F32), jax.ShapeDtypeStruct((S, D), F32),
                             jax.ShapeDtypeStruct((1, D), F32)],
                  compiler_params=_params("arbitrary"))(x, g, tgt)


def _sigmoid(z):
    return 1.0 / (1.0 + jnp.exp(-z))


def _gate_fwd(gl, bg, ya, yb, *, name, tm=512):
    S, D = ya.shape

    def body(za_ref, zb_ref, ba_ref, bb_ref, ya_ref, yb_ref, o_ref):
        ga = _sigmoid(za_ref[...] + ba_ref[...])
        gb = _sigmoid(zb_ref[...] + bb_ref[...])
        o_ref[...] = (ga * ya_ref[...] + gb * yb_ref[...]).astype(o_ref.dtype)

    lo = pl.BlockSpec((tm, D), lambda i: (i, 0))
    hi = pl.BlockSpec((tm, D), lambda i: (i, 1))
    vlo = pl.BlockSpec((1, D), lambda i: (0, 0))
    vhi = pl.BlockSpec((1, D), lambda i: (0, 1))
    return _pcall(body, name=name, grid=(S // tm,), in_specs=[lo, hi, vlo, vhi, lo, lo], out_specs=lo,
                  out_shape=jax.ShapeDtypeStruct((S, D), _CD), compiler_params=_params("parallel"))(gl, gl, bg, bg, ya, yb)


def _gate_bwd(dm, gl, bg, ya, yb, *, name, tm=256):
    S, D = ya.shape

    def body(dm_ref, za_ref, zb_ref, ba_ref, bb_ref, ya_ref, yb_ref, dya_ref, dyb_ref, dgl_ref, dbg_ref):
        dmv = dm_ref[...]
        ga = _sigmoid(za_ref[...] + ba_ref[...])
        gb = _sigmoid(zb_ref[...] + bb_ref[...])
        dya_ref[...] = (dmv * ga).astype(dya_ref.dtype)
        dyb_ref[...] = (dmv * gb).astype(dyb_ref.dtype)
        dza = dmv * ya_ref[...] * ga * (1.0 - ga)
        dzb = dmv * yb_ref[...] * gb * (1.0 - gb)
        dgl_ref[:, :D] = dza.astype(dgl_ref.dtype)
        dgl_ref[:, D:] = dzb.astype(dgl_ref.dtype)
        pa = jnp.sum(dza, axis=0, keepdims=True)
        pb = jnp.sum(dzb, axis=0, keepdims=True)

        @pl.when(pl.program_id(0) == 0)
        def _():
            dbg_ref[:, :D] = pa
            dbg_ref[:, D:] = pb

        @pl.when(pl.program_id(0) > 0)
        def _():
            dbg_ref[:, :D] += pa
            dbg_ref[:, D:] += pb

    lo = pl.BlockSpec((tm, D), lambda i: (i, 0))
    hi = pl.BlockSpec((tm, D), lambda i: (i, 1))
    vlo = pl.BlockSpec((1, D), lambda i: (0, 0))
    vhi = pl.BlockSpec((1, D), lambda i: (0, 1))
    wide = pl.BlockSpec((tm, 2 * D), lambda i: (i, 0))
    vwide = pl.BlockSpec((1, 2 * D), lambda i: (0, 0))
    return _pcall(body, name=name, grid=(S // tm,), in_specs=[lo, lo, hi, vlo, vhi, lo, lo],
                  out_specs=[lo, lo, wide, vwide],
                  out_shape=[jax.ShapeDtypeStruct((S, D), _CD), jax.ShapeDtypeStruct((S, D), _CD),
                             jax.ShapeDtypeStruct((S, 2 * D), _CD), jax.ShapeDtypeStruct((1, 2 * D), F32)],
                  compiler_params=_params("arbitrary"))(dm, gl, gl, bg, bg, ya, yb)


def _swiglu_fwd(gu, *, name, tm=256):
    S, F2 = gu.shape
    F = F2 // 2

    def body(g_ref, u_ref, o_ref):
        gv = g_ref[...]
        o_ref[...] = (gv * _sigmoid(gv) * u_ref[...]).astype(o_ref.dtype)

    lo = pl.BlockSpec((tm, F), lambda i: (i, 0))
    hi = pl.BlockSpec((tm, F), lambda i: (i, 1))
    return _pcall(body, name=name, grid=(S // tm,), in_specs=[lo, hi], out_specs=lo,
                  out_shape=jax.ShapeDtypeStruct((S, F), _CD), compiler_params=_params("parallel"))(gu, gu)


def _swiglu_bwd(dact, gu, *, name, tm=256):
    S, F2 = gu.shape
    F = F2 // 2

    def body(d_ref, g_ref, u_ref, o_ref):
        dv = d_ref[...]
        gv = g_ref[...]
        sg = _sigmoid(gv)
        o_ref[:, :F] = (dv * u_ref[...] * (sg * (1.0 + gv * (1.0 - sg)))).astype(o_ref.dtype)
        o_ref[:, F:] = (dv * (gv * sg)).astype(o_ref.dtype)

    lo = pl.BlockSpec((tm, F), lambda i: (i, 0))
    hi = pl.BlockSpec((tm, F), lambda i: (i, 1))
    return _pcall(body, name=name, grid=(S // tm,), in_specs=[lo, lo, hi],
                  out_specs=pl.BlockSpec((tm, F2), lambda i: (i, 0)),
                  out_shape=jax.ShapeDtypeStruct((S, F2), _CD), compiler_params=_params("parallel"))(dact, gu, gu)


def _split3(x):
    hi = x.astype(jnp.bfloat16)
    r1 = x - hi.astype(F32)
    mid = r1.astype(jnp.bfloat16)
    lo = (r1 - mid.astype(F32)).astype(jnp.bfloat16)
    return hi, mid, lo


def _ones_dot_left(ones, x):
    return sum(jnp.dot(ones, p, preferred_element_type=F32) for p in _split3(x))


def _ones_dot_right(x, ones):
    return sum(jnp.dot(p, ones, preferred_element_type=F32) for p in _split3(x))


def _head_sum(x):
    n = x.shape[1]
    r = lax.broadcasted_iota(jnp.int32, (n, n), 0) // HEAD_DIM
    c = lax.broadcasted_iota(jnp.int32, (n, n), 1) // HEAD_DIM
    return _ones_dot_right(x, (r == c).astype(jnp.bfloat16))


def _log_sigmoid(z):
    e = jnp.exp(-jnp.abs(z))
    t = 1.0 + e
    log1p_e = jnp.where(t == 1.0, e, jnp.log(t) * (e / jnp.where(t == 1.0, 1.0, t - 1.0)))
    return jnp.minimum(z, 0.0) - log1p_e


def _fox_cumsum(zf, bf, *, name):
    S, W = zf.shape
    nb = S // 128

    def body(z_ref, b_ref, c_ref):
        tri = (lax.broadcasted_iota(jnp.int32, (128, 128), 0) >= lax.broadcasted_iota(jnp.int32, (128, 128), 1))
        tri = tri.astype(jnp.bfloat16)

        def step(i, carry):
            rows = pl.ds(pl.multiple_of(i * 128, 128), 128)
            lf = _log_sigmoid(z_ref[rows, :] + b_ref[...])
            cb = _ones_dot_left(tri, lf) + carry
            c_ref[rows, :] = cb
            return cb[127:128, :]

        lax.fori_loop(0, nb, step, jnp.zeros((1, W), F32))

    return _pcall(body, name=name, out_shape=jax.ShapeDtypeStruct((S, W), F32),
                  compiler_params=pltpu.CompilerParams(vmem_limit_bytes=VMEM_LIMIT))(zf, bf)


def _fox_cumsum_bwd(dc, zf, bf, *, name):
    S, W = zf.shape
    nb = S // 128

    def body(dc_ref, z_ref, b_ref, dz_ref, db_ref):
        tri = (lax.broadcasted_iota(jnp.int32, (128, 128), 0) <= lax.broadcasted_iota(jnp.int32, (128, 128), 1))
        tri = tri.astype(jnp.bfloat16)

        def step(k, carry):
            tail, acc = carry
            i = nb - 1 - k
            rows = pl.ds(pl.multiple_of(i * 128, 128), 128)
            dlf = _ones_dot_left(tri, dc_ref[rows, :]) + tail
            dz = dlf * _sigmoid(-(z_ref[rows, :] + b_ref[...]))
            dz_ref[rows, :] = dz
            return dlf[0:1, :], acc + jnp.sum(dz, axis=0, keepdims=True)

        _, acc = lax.fori_loop(0, nb, step, (jnp.zeros((1, W), F32), jnp.zeros((1, W), F32)))
        db_ref[...] = acc

    return _pcall(body, name=name,
                  out_shape=[jax.ShapeDtypeStruct((S, W), F32), jax.ShapeDtypeStruct((1, W), F32)],
                  compiler_params=pltpu.CompilerParams(vmem_limit_bytes=VMEM_LIMIT))(dc, zf, bf)


def _dil_slopes(group):
    h = np.arange(1, N_DIL_GROUPS * DIL_HEADS + 1, dtype=np.float32)
    s = (np.float32(2.0) ** (np.float32(-8.0) * h / np.float32(N_DIL_GROUPS * DIL_HEADS))).astype(np.float32)
    return [float(v) for v in s.reshape(N_DIL_GROUPS, DIL_HEADS)[group]]


def _dil_tiles(i, n, blocks_per_seq):
    qi = lax.broadcasted_iota(jnp.int32, (DIL_W, DIL_W), 0)
    kj = lax.broadcasted_iota(jnp.int32, (DIL_W, DIL_W), 1)
    first = ((4 * n + i) % blocks_per_seq) == 0
    valid_prev = jnp.logical_and(kj >= qi, jnp.logical_not(first))
    valid_cur = kj <= qi
    rel_prev = (qi - kj + DIL_W).astype(F32)
    rel_cur = (qi - kj).astype(F32)
    return valid_prev, valid_cur, rel_prev, rel_cur


CHUNK = 4 * DIL_W


def _dil_fwd(q, k, v, group, *, name):
    S = q.shape[0]
    dilation = DIL_PAIRS[group][1]
    bps = (S // dilation) // DIL_W
    slopes = _dil_slopes(group)
    nt = (((1,), (1,)), ((), ()))

    def body(q_ref, k_ref, v_ref, kp_ref, vp_ref, o_ref, l_ref):
        n = pl.program_id(0)
        for i in range(4):
            valid_prev, valid_cur, rel_prev, rel_cur = _dil_tiles(i, n, bps)
            rows = slice(i * DIL_W, (i + 1) * DIL_W)
            prow = slice((i - 1) * DIL_W, i * DIL_W)
            for h in range(DIL_HEADS):
                cols = slice(h * HEAD_DIM, (h + 1) * HEAD_DIM)
                qh = q_ref[rows, cols]
                kc, vc = k_ref[rows, cols], v_ref[rows, cols]
                kp = kp_ref[:, cols] if i == 0 else k_ref[prow, cols]
                vp = vp_ref[:, cols] if i == 0 else v_ref[prow, cols]
                sl = slopes[h] * dilation
                sp = lax.dot_general(qh, kp, nt, preferred_element_type=F32) * ATTN_SCALE - sl * rel_prev
                sc = lax.dot_general(qh, kc, nt, preferred_element_type=F32) * ATTN_SCALE - sl * rel_cur
                sp = jnp.where(valid_prev, sp, NEG_INF)
                sc = jnp.where(valid_cur, sc, NEG_INF)
                m = jnp.maximum(jnp.max(sp, axis=-1, keepdims=True), jnp.max(sc, axis=-1, keepdims=True))
                pp, pc = jnp.exp(sp - m), jnp.exp(sc - m)
                den = jnp.sum(pp, axis=-1, keepdims=True) + jnp.sum(pc, axis=-1, keepdims=True)
                acc = (jnp.dot(pp.astype(_CD), vp, preferred_element_type=F32)
                       + jnp.dot(pc.astype(_CD), vc, preferred_element_type=F32))
                o_ref[rows, cols] = acc / den
                l_ref[rows, cols] = jnp.broadcast_to(m + jnp.log(den), (DIL_W, HEAD_DIM))

    cur = pl.BlockSpec((CHUNK, DIL_OUT), lambda n: (n, 0))
    prev = pl.BlockSpec((DIL_W, DIL_OUT), lambda n: (jnp.maximum(4 * n - 1, 0), 0))
    return _pcall(body, name=name, grid=(S // CHUNK,), in_specs=[cur, cur, cur, prev, prev], out_specs=[cur, cur],
                  out_shape=[jax.ShapeDtypeStruct((S, DIL_OUT), F32), jax.ShapeDtypeStruct((S, DIL_OUT), F32)],
                  compiler_params=_params("parallel"))(q, k, v, k, v)


def _dil_bwd(q, k, v, o, lse, do, dlse, group, *, name):
    S = q.shape[0]
    dilation = DIL_PAIRS[group][1]
    bps = (S // dilation) // DIL_W
    slopes = _dil_slopes(group)
    nchunk = S // CHUNK
    nt = (((1,), (1,)), ((), ()))
    tn = (((0,), (0,)), ((), ()))

    def body(q_ref, k_ref, v_ref, kp_ref, vp_ref, o_ref, l_ref, do_ref, dl_ref, dq_ref, dk_ref, dv_ref,
             dk_s, dv_s):
        step = pl.program_id(0)
        n = nchunk - 1 - step

        @pl.when(step == 0)
        def _():
            dk_s[CHUNK:, :] = jnp.zeros((DIL_W, DIL_OUT), F32)
            dv_s[CHUNK:, :] = jnp.zeros((DIL_W, DIL_OUT), F32)

        dk_s[:CHUNK, :] = jnp.zeros((CHUNK, DIL_OUT), F32)
        dv_s[:CHUNK, :] = jnp.zeros((CHUNK, DIL_OUT), F32)
        for i in range(4):
            valid_prev, valid_cur, rel_prev, rel_cur = _dil_tiles(i, n, bps)
            rows = slice(i * DIL_W, (i + 1) * DIL_W)
            prow = slice((i - 1) * DIL_W, i * DIL_W)
            s_prev = slice(i * DIL_W, (i + 1) * DIL_W)
            s_cur = slice((i + 1) * DIL_W, (i + 2) * DIL_W)
            for h in range(DIL_HEADS):
                cols = slice(h * HEAD_DIM, (h + 1) * HEAD_DIM)
                qh = q_ref[rows, cols]
                kc, vc = k_ref[rows, cols], v_ref[rows, cols]
                kp = kp_ref[:, cols] if i == 0 else k_ref[prow, cols]
                vp = vp_ref[:, cols] if i == 0 else v_ref[prow, cols]
                sl = slopes[h] * dilation
                lh = l_ref[rows, h * HEAD_DIM:h * HEAD_DIM + 1]
                sp = lax.dot_general(qh, kp, nt, preferred_element_type=F32) * ATTN_SCALE - sl * rel_prev
                sc = lax.dot_general(qh, kc, nt, preferred_element_type=F32) * ATTN_SCALE - sl * rel_cur
                pp = jnp.exp(jnp.where(valid_prev, sp, NEG_INF) - lh)
                pc = jnp.exp(jnp.where(valid_cur, sc, NEG_INF) - lh)
                doh = do_ref[rows, cols]
                dsum = jnp.sum(doh * o_ref[rows, cols], axis=-1, keepdims=True)
                shift = dl_ref[rows, h * HEAD_DIM:h * HEAD_DIM + 1] - dsum
                dob = doh.astype(_CD)
                dsp = pp * (lax.dot_general(dob, vp, nt, preferred_element_type=F32) + shift)
                dsc = pc * (lax.dot_general(dob, vc, nt, preferred_element_type=F32) + shift)
                dspb = (dsp * ATTN_SCALE).astype(_CD)
                dscb = (dsc * ATTN_SCALE).astype(_CD)
                dq_ref[rows, cols] = (jnp.dot(dspb, kp, preferred_element_type=F32)
                                      + jnp.dot(dscb, kc, preferred_element_type=F32)).astype(dq_ref.dtype)
                dk_s[s_prev, cols] += lax.dot_general(dspb, qh, tn, preferred_element_type=F32)
                dk_s[s_cur, cols] += lax.dot_general(dscb, qh, tn, preferred_element_type=F32)
                dv_s[s_prev, cols] += lax.dot_general(pp.astype(_CD), dob, tn, preferred_element_type=F32)
                dv_s[s_cur, cols] += lax.dot_general(pc.astype(_CD), dob, tn, preferred_element_type=F32)
        dk_ref[...] = dk_s[DIL_W:, :].astype(dk_ref.dtype)
        dv_ref[...] = dv_s[DIL_W:, :].astype(dv_ref.dtype)
        dk_s[CHUNK:, :] = dk_s[:DIL_W, :]
        dv_s[CHUNK:, :] = dv_s[:DIL_W, :]

    cur = pl.BlockSpec((CHUNK, DIL_OUT), lambda s: (nchunk - 1 - s, 0))
    prev = pl.BlockSpec((DIL_W, DIL_OUT), lambda s: (jnp.maximum(4 * (nchunk - 1 - s) - 1, 0), 0))
    shp = jax.ShapeDtypeStruct((S, DIL_OUT), _CD)
    return _pcall(body, name=name, grid=(nchunk,), in_specs=[cur, cur, cur, prev, prev, cur, cur, cur, cur],
                  out_specs=[cur, cur, cur], out_shape=[shp, shp, shp],
                  scratch_shapes=[pltpu.VMEM((CHUNK + DIL_W, DIL_OUT), F32), pltpu.VMEM((CHUNK + DIL_W, DIL_OUT), F32)],
                  compiler_params=_params("arbitrary"))(q, k, v, k, v, o, lse, do, dlse)


def _dil_mix_fwd(os_, ls_, *, name, tm=512):
    S, W = os_[0].shape

    def body(o0, o1, o2, l0, l1, l2, out_ref):
        ls = [l0[...], l1[...], l2[...]]
        m = jnp.maximum(jnp.maximum(ls[0], ls[1]), ls[2])
        es = [jnp.exp(l - m) for l in ls]
        den = es[0] + es[1] + es[2]
        out_ref[...] = ((es[0] * o0[...] + es[1] * o1[...] + es[2] * o2[...]) / den).astype(out_ref.dtype)

    row = pl.BlockSpec((tm, W), lambda i: (i, 0))
    return _pcall(body, name=name, grid=(S // tm,), in_specs=[row] * 6, out_specs=row,
                  out_shape=jax.ShapeDtypeStruct((S, W), _CD), compiler_params=_params("parallel"))(*os_, *ls_)


def _dil_mix_bwd(doa, os_, ls_, *, name, tm=512):
    S, W = doa.shape

    def body(d_ref, o0, o1, o2, l0, l1, l2, do0, do1, do2, dl0, dl1, dl2):
        dv = d_ref[...]
        ls = [l0[...], l1[...], l2[...]]
        m = jnp.maximum(jnp.maximum(ls[0], ls[1]), ls[2])
        es = [jnp.exp(l - m) for l in ls]
        den = es[0] + es[1] + es[2]
        al = [e / den for e in es]
        da = [_head_sum(dv * o[...]) for o in (o0, o1, o2)]
        mean = al[0] * da[0] + al[1] * da[1] + al[2] * da[2]
        for a, d_, do_ref, dl_ref in zip(al, da, (do0, do1, do2), (dl0, dl1, dl2)):
            do_ref[...] = a * dv
            dl_ref[...] = a * (d_ - mean)

    row = pl.BlockSpec((tm, W), lambda i: (i, 0))
    shp = jax.ShapeDtypeStruct((S, W), F32)
    return _pcall(body, name=name, grid=(S // tm,), in_specs=[row] * 7, out_specs=[row] * 6, out_shape=[shp] * 6,
                  compiler_params=_params("parallel"))(doa, *os_, *ls_)


FOX_T = 512


PACK = 2 * HEAD_DIM
HEAD_PAIRS = N_FOX_HEADS // 2
Q_BLOCK0 = (3 * DIL_WIDTH) // PACK
K_BLOCK0 = (3 * DIL_WIDTH + FOX_WIDTH) // PACK
V_BLOCK0 = (3 * DIL_WIDTH + 2 * FOX_WIDTH) // PACK


def _pieces(x):
    hi = x.astype(jnp.bfloat16).astype(F32)
    r = x - hi
    mid = r.astype(jnp.bfloat16).astype(F32)
    lo = (r - mid).astype(jnp.bfloat16).astype(F32)
    return [hi, mid, lo]


def _extras(first, second, rows):
    lane = lax.broadcasted_iota(jnp.int32, (rows, HEAD_DIM), 1)
    out = jnp.zeros((rows, HEAD_DIM), F32)
    for idx, val in enumerate(list(first) + list(second)):
        out = jnp.where(lane == idx, val, out)
    return out


def _head_column(c, h):
    lane = lax.broadcasted_iota(jnp.int32, c.shape, 1)
    return jnp.sum(jnp.where(lane == h, c, 0.0), axis=1, keepdims=True)


ONES3 = [1.0, 1.0, 1.0]
ZEROS3 = [0.0, 0.0, 0.0]


def _fox_pack_fwd(qkv, c, *, name, tm=512):
    S = qkv.shape[0]

    def body(q_ref, k_ref, v_ref, c_ref, qo_ref, ko_ref, vo_ref):
        hp = pl.program_id(1)
        cv = c_ref[...]
        for hh in range(2):
            ch = _pieces(_head_column(cv, 2 * hp + hh))
            src = slice(hh * HEAD_DIM, (hh + 1) * HEAD_DIM)
            lo = slice(hh * PACK, hh * PACK + HEAD_DIM)
            hi = slice(hh * PACK + HEAD_DIM, (hh + 1) * PACK)
            qo_ref[:, lo] = (q_ref[:, src].astype(F32) * ATTN_SCALE).astype(qo_ref.dtype)
            qo_ref[:, hi] = _extras(ch, ONES3, tm).astype(qo_ref.dtype)
            ko_ref[:, lo] = k_ref[:, src]
            ko_ref[:, hi] = _extras(ONES3, [-p for p in ch], tm).astype(ko_ref.dtype)
            vo_ref[:, lo] = v_ref[:, src]
            vo_ref[:, hi] = _extras(ONES3, ZEROS3, tm).astype(vo_ref.dtype)

    def src(block0):
        return pl.BlockSpec((tm, PACK), lambda i, hp: (i, block0 + hp))

    out = pl.BlockSpec((tm, 2 * PACK), lambda i, hp: (i, hp))
    shp = jax.ShapeDtypeStruct((S, N_FOX_HEADS * PACK), _CD)
    return _pcall(body, name=name, grid=(S // tm, HEAD_PAIRS),
                  in_specs=[src(Q_BLOCK0), src(K_BLOCK0), src(V_BLOCK0), pl.BlockSpec((tm, PACK), lambda i, hp: (i, 0))],
                  out_specs=[out, out, out], out_shape=[shp, shp, shp],
                  compiler_params=_params("parallel", "parallel"))(qkv, qkv, qkv, c)


def _fox_fwd(qp, kp, vp, *, name):
    S = qp.shape[0]
    nt = S // FOX_T
    nt_dims = (((1,), (1,)), ((), ()))
    tn_dims = (((0,), (0,)), ((), ()))

    def body(q_ref, k_ref, v_ref, o_ref, l_ref, m_s, acc_s):
        i, j = pl.program_id(1), pl.program_id(2)

        @pl.when(j == 0)
        def _():
            m_s[...] = jnp.full((2, 1, FOX_T), NEG_INF, F32)
            acc_s[...] = jnp.zeros((2, PACK, FOX_T), F32)

        def tile(diagonal):
            for hh in range(2):
                cols = slice(hh * PACK, (hh + 1) * PACK)
                st = lax.dot_general(k_ref[:, cols], q_ref[:, cols], nt_dims, preferred_element_type=F32)
                if diagonal:
                    key = lax.broadcasted_iota(jnp.int32, (FOX_T, FOX_T), 0)
                    qry = lax.broadcasted_iota(jnp.int32, (FOX_T, FOX_T), 1)
                    st = jnp.where(key <= qry, st, NEG_INF)
                m_old = m_s[hh]
                m_new = jnp.maximum(m_old, jnp.max(st, axis=0, keepdims=True))
                pt = jnp.exp(st - m_new)
                acc_s[hh] = jnp.exp(m_old - m_new) * acc_s[hh] + lax.dot_general(
                    v_ref[:, cols], pt.astype(_CD), tn_dims, preferred_element_type=F32)
                m_s[hh] = m_new

        @pl.when(j < i)
        def _():
            tile(False)

        @pl.when(j == i)
        def _():
            tile(True)

        @pl.when(j == nt - 1)
        def _():
            for hh in range(2):
                acc = acc_s[hh]
                den = acc[HEAD_DIM:HEAD_DIM + 1, :]
                cols = slice(hh * HEAD_DIM, (hh + 1) * HEAD_DIM)
                o_ref[:, cols] = (acc[:HEAD_DIM, :] / den).T
                l_ref[:, cols] = jnp.broadcast_to(m_s[hh] + jnp.log(den), (HEAD_DIM, FOX_T)).T

    qs = pl.BlockSpec((FOX_T, 2 * PACK), lambda hp, i, j: (i, hp))
    ks = pl.BlockSpec((FOX_T, 2 * PACK), lambda hp, i, j: (jnp.minimum(i, j), hp))
    os_ = pl.BlockSpec((FOX_T, PACK), lambda hp, i, j: (i, hp))
    shp = jax.ShapeDtypeStruct((S, FOX_WIDTH), F32)
    return _pcall(body, name=name, grid=(HEAD_PAIRS, nt, nt), in_specs=[qs, ks, ks], out_specs=[os_, os_],
                  out_shape=[shp, shp],
                  scratch_shapes=[pltpu.VMEM((2, 1, FOX_T), F32), pltpu.VMEM((2, PACK, FOX_T), F32)],
                  compiler_params=_params("parallel", "parallel", "arbitrary"))(qp, kp, vp)


def _fox_pack_bwd(qkv, c, o, lse, do, *, name, tm=512):
    S = qkv.shape[0]

    def body(q_ref, c_ref, o_ref, l_ref, do_ref, qo_ref, do_out_ref):
        hp = pl.program_id(1)
        cv = c_ref[...]
        for hh in range(2):
            src = slice(hh * HEAD_DIM, (hh + 1) * HEAD_DIM)
            lo = slice(hh * PACK, hh * PACK + HEAD_DIM)
            hi = slice(hh * PACK + HEAD_DIM, (hh + 1) * PACK)
            shift = _head_column(cv, 2 * hp + hh) - l_ref[:, hh * HEAD_DIM:hh * HEAD_DIM + 1]
            dov = do_ref[:, src]
            dsum = jnp.sum(dov * o_ref[:, src], axis=-1, keepdims=True)
            qo_ref[:, lo] = (q_ref[:, src].astype(F32) * ATTN_SCALE).astype(qo_ref.dtype)
            qo_ref[:, hi] = _extras(_pieces(shift), ONES3, tm).astype(qo_ref.dtype)
            do_out_ref[:, lo] = dov.astype(do_out_ref.dtype)
            do_out_ref[:, hi] = _extras(_pieces(-dsum), ZEROS3, tm).astype(do_out_ref.dtype)

    pair = pl.BlockSpec((tm, PACK), lambda i, hp: (i, hp))
    out = pl.BlockSpec((tm, 2 * PACK), lambda i, hp: (i, hp))
    shp = jax.ShapeDtypeStruct((S, N_FOX_HEADS * PACK), _CD)
    return _pcall(body, name=name, grid=(S // tm, HEAD_PAIRS),
                  in_specs=[pl.BlockSpec((tm, PACK), lambda i, hp: (i, Q_BLOCK0 + hp)),
                            pl.BlockSpec((tm, PACK), lambda i, hp: (i, 0)), pair, pair, pair],
                  out_specs=[out, out], out_shape=[shp, shp],
                  compiler_params=_params("parallel", "parallel"))(qkv, c, o, lse, do)


def _fox_bwd(qp, kp, vp, dop, *, name):
    S = qp.shape[0]
    nt = S // FOX_T
    nt_dims = (((1,), (1,)), ((), ()))
    tn_dims = (((0,), (0,)), ((), ()))

    def body(q_ref, k_ref, v_ref, do_ref, dq_ref, dk_ref, dv_ref, dc_ref, dr_ref, dq_s, dk_s, dv_s, dc_s, dr_s):
        j, i = pl.program_id(1), pl.program_id(2)

        @pl.when(jnp.logical_and(j == 0, i == 0))
        def _():
            dq_s[...] = jnp.zeros((S, 2 * PACK), F32)
            dr_s[...] = jnp.zeros((2, 1, S), F32)

        @pl.when(i == 0)
        def _():
            dk_s[...] = jnp.zeros((FOX_T, 2 * PACK), F32)
            dv_s[...] = jnp.zeros((FOX_T, 2 * PACK), F32)
            dc_s[...] = jnp.zeros((2, FOX_T, 1), F32)

        def tile(diagonal):
            rows = pl.ds(pl.multiple_of(i * FOX_T, FOX_T), FOX_T)
            for hh in range(2):
                cols = slice(hh * PACK, (hh + 1) * PACK)
                qv, kv, vv, dov = q_ref[:, cols], k_ref[:, cols], v_ref[:, cols], do_ref[:, cols]
                pt = jnp.exp(lax.dot_general(kv, qv, nt_dims, preferred_element_type=F32))
                if diagonal:
                    key = lax.broadcasted_iota(jnp.int32, (FOX_T, FOX_T), 0)
                    qry = lax.broadcasted_iota(jnp.int32, (FOX_T, FOX_T), 1)
                    pt = jnp.where(key <= qry, pt, 0.0)
                dst = pt * lax.dot_general(vv, dov, nt_dims, preferred_element_type=F32)
                dsb = dst.astype(_CD)
                dc_s[hh] += jnp.sum(dst, axis=1, keepdims=True)
                dr_s[hh, :, rows] += jnp.sum(dst, axis=0, keepdims=True)
                dv_s[:, cols] += jnp.dot(pt.astype(_CD), dov, preferred_element_type=F32)
                dk_s[:, cols] += jnp.dot(dsb, qv, preferred_element_type=F32)
                dq_s[rows, cols] += lax.dot_general(dsb, kv, tn_dims, preferred_element_type=F32)

        @pl.when(i > j)
        def _():
            tile(False)

        @pl.when(i == j)
        def _():
            tile(True)

        @pl.when(i == nt - 1)
        def _():
            dk_ref[...] = dk_s[...].astype(dk_ref.dtype)
            dv_ref[...] = dv_s[...].astype(dv_ref.dtype)
            for hh in range(2):
                dc_ref[:, hh * HEAD_DIM:(hh + 1) * HEAD_DIM] = jnp.broadcast_to(dc_s[hh], (FOX_T, HEAD_DIM))

        @pl.when(jnp.logical_and(j == nt - 1, i == nt - 1))
        def _():
            lane = lax.broadcasted_iota(jnp.int32, (S, 2 * PACK), 1) % PACK
            dq_ref[...] = (dq_s[...] * jnp.where(lane < HEAD_DIM, ATTN_SCALE, 1.0)).astype(dq_ref.dtype)
            dr_ref[...] = dr_s[...]

    qs = pl.BlockSpec((FOX_T, 2 * PACK), lambda hp, j, i: (jnp.maximum(i, j), hp))
    ks = pl.BlockSpec((FOX_T, 2 * PACK), lambda hp, j, i: (j, hp))
    whole = pl.BlockSpec((S, 2 * PACK), lambda hp, j, i: (0, hp))
    cs = pl.BlockSpec((FOX_T, PACK), lambda hp, j, i: (j, hp))
    rs = pl.BlockSpec((2, 1, S), lambda hp, j, i: (hp, 0, 0))
    shp = jax.ShapeDtypeStruct((S, N_FOX_HEADS * PACK), _CD)
    return _pcall(body, name=name, grid=(HEAD_PAIRS, nt, nt), in_specs=[qs, ks, ks, qs],
                  out_specs=[whole, ks, ks, cs, rs],
                  out_shape=[shp, shp, shp, jax.ShapeDtypeStruct((S, FOX_WIDTH), F32),
                             jax.ShapeDtypeStruct((N_FOX_HEADS, 1, S), F32)],
                  scratch_shapes=[pltpu.VMEM((S, 2 * PACK), F32), pltpu.VMEM((FOX_T, 2 * PACK), F32),
                                  pltpu.VMEM((FOX_T, 2 * PACK), F32), pltpu.VMEM((2, FOX_T, 1), F32),
                                  pltpu.VMEM((2, 1, S), F32)],
                  compiler_params=_params("parallel", "arbitrary", "arbitrary"))(qp, kp, vp, dop)


def _dedilate(t, d):
    if d == 1:
        return t
    S, C = t.shape
    return t.reshape(S // d, d, C).transpose(1, 0, 2).reshape(S, C)


def _redilate(t, d):
    if d == 1:
        return t
    S, C = t.shape
    return t.reshape(d, S // d, C).transpose(1, 0, 2).reshape(S, C)


def _layer_step(x, tgt, w, p, late_weights=None, grad_sink=None, after=None):
    S = x.shape[0]
    h = _rms_fwd(x, p["norm_mix_g"], name="rms_mix", after=after)
    qkv = _mm(h, w["qkv"], name="proj_qkv", out_dtype=_CD, tn=768)
    zf = _mm(h, w["f"], name="proj_f")
    gl = _mm(h, w["g"], name="proj_gate", tn=1024)

    dil_q, dil_k, dil_v = [], [], []
    dil_o, dil_l = [], []
    for g, (_, d) in enumerate(DIL_PAIRS):
        qg = _dedilate(qkv[:, g * DIL_OUT:(g + 1) * DIL_OUT], d)
        kg = _dedilate(qkv[:, DIL_WIDTH + g * DIL_OUT:DIL_WIDTH + (g + 1) * DIL_OUT], d)
        vg = _dedilate(qkv[:, 2 * DIL_WIDTH + g * DIL_OUT:2 * DIL_WIDTH + (g + 1) * DIL_OUT], d)
        og, lg = _dil_fwd(qg, kg, vg, g, name=f"dil_fwd{g}")
        dil_q.append(qg), dil_k.append(kg), dil_v.append(vg)
        dil_o.append(_redilate(og, d)), dil_l.append(_redilate(lg, d))
    o_a = _dil_mix_fwd(dil_o, dil_l, name="dil_mix")

    c = _fox_cumsum(zf, p["b_fgt"], name="fox_cumsum")
    fqp, fkp, fvp = _fox_pack_fwd(qkv, c, name="fox_pack")
    o_b, flse = _fox_fwd(fqp, fkp, fvp, name="fox_fwd")

    if late_weights is not None:
        w = {**w, **late_weights(o_b)}
    y_a = _mm(o_a, w["dil_out"], name="y_a", tn=1024)
    y_b = _mm(o_b, w["fox_out"], name="y_b", tn=1024)
    merged = _gate_fwd(gl, p["b_gate"], y_a, y_b, name="gate_fwd")
    x1 = _mm(merged, w["out"], name="mix_out", add=x)

    h2 = _rms_fwd(x1, p["norm_ffn_g"], name="rms_ffn")
    gu = _mm(h2, w["ffn_in"], name="ffn_in", tn=1408)
    act = _swiglu_fwd(gu, name="swiglu")
    x2 = _mm(act, w["ffn_down"], name="ffn_down", add=x1, tk=2816)

    loss, dx2, dg_final = _loss_head(x2, p["norm_final_g"], tgt, name="loss_head")

    dact = _mm(dx2, w["ffn_down"], name="d_act", tb=True, tn=1408)
    gw_ffn_down = _mm(act, dx2, name="gw_ffn_down", ta=True, out_dtype=_CD, tm=1408)
    dgu = _swiglu_bwd(dact, gu, name="swiglu_bwd")
    dh2 = _mm(dgu, w["ffn_in"], name="d_h2", tb=True, tk=2816)
    gw_ffn_in = _mm(h2, dgu, name="gw_ffn_in", ta=True, out_dtype=_CD)
    if grad_sink is not None:
        grad_sink("ffn", dict(ffn_in=gw_ffn_in, ffn_down=gw_ffn_down))
    dx1, dg_ffn = _rms_bwd(x1, p["norm_ffn_g"], dh2, dx2, name="rms_ffn_bwd")

    dmerged = _mm(dx1, w["out"], name="d_merged", tb=True)
    gw_out = _mm(merged, dx1, name="gw_out", ta=True, out_dtype=_CD)
    dy_a, dy_b, dgl, db_gate = _gate_bwd(dmerged, gl, p["b_gate"], y_a, y_b, name="gate_bwd")
    do_a = _mm(dy_a, w["dil_out"], name="d_o_a", tb=True)
    gw_dil_out = _mm(o_a, dy_a, name="gw_dil_out", ta=True, out_dtype=_CD, tn=1024)
    do_b = _mm(dy_b, w["fox_out"], name="d_o_b", tb=True)
    gw_fox_out = _mm(o_b, dy_b, name="gw_fox_out", ta=True, out_dtype=_CD, tn=1024)
    if grad_sink is not None:
        grad_sink("mix", dict(dil_out=gw_dil_out, fox_out=gw_fox_out, out=gw_out))

    bqp, bdop = _fox_pack_bwd(qkv, c, o_b, flse, do_b, name="fox_pack_bwd")
    dqp, dkp, dvp, dck, dcq = _fox_bwd(bqp, fkp, fvp, bdop, name="fox_bwd")
    dc = dcq[:, 0, :].T - dck.reshape(S, N_FOX_HEADS, HEAD_DIM)[:, :, 0]
    dc = jnp.pad(dc, ((0, 0), (0, F_PAD - N_FOX_HEADS)))
    dzf, db_fgt = _fox_cumsum_bwd(dc, zf, p["b_fgt"], name="fox_cumsum_bwd")

    def unpack(t):
        return t.reshape(S, N_FOX_HEADS, PACK)[:, :, :HEAD_DIM].reshape(S, FOX_WIDTH)

    douts = _dil_mix_bwd(do_a, dil_o, dil_l, name="dil_mix_bwd")
    dqs, dks, dvs = [], [], []
    for g, (_, d) in enumerate(DIL_PAIRS):
        dq, dk, dv = _dil_bwd(dil_q[g], dil_k[g], dil_v[g], _dedilate(dil_o[g], d), _dedilate(dil_l[g], d),
                              _dedilate(douts[g], d), _dedilate(douts[3 + g], d), g, name=f"dil_bwd{g}")
        dqs.append(_redilate(dq, d)), dks.append(_redilate(dk, d)), dvs.append(_redilate(dv, d))
    dqkv = jnp.concatenate(dqs + dks + dvs + [unpack(dqp), unpack(dkp), unpack(dvp)], axis=1)

    dh = _mm(dqkv, w["qkv"], name="d_h_qkv", tb=True, tk=1920)
    dh = _mm(dgl, w["g"], name="d_h_gate", tb=True, add=dh)
    dh = _mm(dzf, w["f"], name="d_h_f", tb=True, add=dh)
    gw_qkv = _mm(h, dqkv, name="gw_qkv", ta=True, out_dtype=_CD, tn=768)
    gw_g = _mm(h, dgl, name="gw_gate", ta=True, out_dtype=_CD)
    gw_f = _mm(h, dzf, name="gw_f", ta=True, out_dtype=_CD)
    dx, dg_mix = _rms_bwd(x, p["norm_mix_g"], dh, dx1, name="rms_mix_bwd")

    gw = dict(qkv=gw_qkv, f=gw_f, g=gw_g, dil_out=gw_dil_out, fox_out=gw_fox_out, out=gw_out, ffn_in=gw_ffn_in,
              ffn_down=gw_ffn_down)
    small = dict(norm_mix_g=dg_mix, b_fgt=db_fgt, b_gate=db_gate, norm_ffn_g=dg_ffn, norm_final_g=dg_final)
    return loss, dx, gw, small


def _position():
    return lax.axis_index("x"), lax.axis_index("y"), lax.axis_index("c")


def _other_chips(x, y):
    return [(1 - x, y), (x, 1 - y), (1 - x, 1 - y)]


def _gather_weights(shards):
    n = len(shards)

    def body(*refs):
        ins, outs = refs[:n], refs[n:2 * n]
        send_sems, recv_sems = refs[2 * n:]
        x, y, c = _position()
        mine = 2 * x + y
        chips = _other_chips(x, y)
        started = []
        for w in range(n):
            half = ins[w].shape[0] // 2
            rows = pl.ds(c * half, half)
            for r, (cx, cy) in enumerate(chips):
                cp = pltpu.make_async_remote_copy(
                    src_ref=ins[w].at[rows, :], dst_ref=outs[w].at[mine, rows, :], send_sem=send_sems.at[w, r],
                    recv_sem=recv_sems.at[w, r], device_id=(cx, cy, c), device_id_type=MESH)
                cp.start()
                started.append(cp)

        def landed(w, r, rows, peer):
            cx, cy = chips[r % 3]
            blk = outs[w].at[2 * cx + cy, rows, :]
            return pltpu.make_async_remote_copy(src_ref=blk, dst_ref=blk, send_sem=send_sems.at[w, r],
                                                recv_sem=recv_sems.at[w, r], device_id=peer, device_id_type=MESH)

        for w in range(n):
            half = ins[w].shape[0] // 2
            rows = pl.ds(c * half, half)
            for r, (cx, cy) in enumerate(chips):
                landed(w, r, rows, (cx, cy, c)).wait_recv()
                fwd = landed(w, 3 + r, rows, (x, y, 1 - c))
                fwd.start()
                started.append(fwd)
        for w in range(n):
            half = ins[w].shape[0] // 2
            rows = pl.ds((1 - c) * half, half)
            for r in range(3):
                landed(w, 3 + r, rows, (x, y, 1 - c)).wait_recv()
        for cp in started:
            cp.wait_send()

    return _pcall(
        body, name="gather_weights", in_specs=[HBM_SPEC] * n, out_specs=[HBM_SPEC] * n,
        out_shape=[jax.ShapeDtypeStruct((4,) + s.shape, s.dtype) for s in shards],
        scratch_shapes=[pltpu.SemaphoreType.DMA((n, 6)), pltpu.SemaphoreType.DMA((n, 6))],
    )(*shards)


SEM_SPEC = pl.BlockSpec(memory_space=pltpu.SEMAPHORE)
ANY_SPEC = pl.BlockSpec(memory_space=pl.ANY)
DATAFLOW = pltpu.SideEffectType.DATAFLOW_SIDE_EFFECTING


def _in_hbm(a):
    return pltpu.with_memory_space_constraint(a, pltpu.HBM)


def _split_copy_start(srcs, land_shapes, copies, after, *, name):
    n, m = len(srcs), len(land_shapes)

    def body(*refs):
        src_refs, land_refs = refs[:n], refs[n:n + m]
        send_sems, recv_sems = refs[n + m + 1], refs[n + m + 2]
        token = refs[-1]
        x, y, c = _position()
        for k, (src, dst, peer) in enumerate(copies(x, y, c, src_refs, land_refs)):
            pltpu.make_async_remote_copy(src_ref=src, dst_ref=dst, send_sem=send_sems.at[k], recv_sem=recv_sems.at[k],
                                         device_id=peer, device_id_type=MESH).start()
        token[...] = jnp.zeros_like(token)

    count = len(copies(0, 0, 0, [None] * n, [None] * m, dry=True))
    lands = [lax.empty(s.shape, s.dtype) for s in land_shapes]
    out = _pcall(
        body, name=name,
        out_shape=(pltpu.SemaphoreType.DMA((count,)), pltpu.SemaphoreType.DMA((count,)),
                   *[pltpu.HBM(s.shape, s.dtype) for s in srcs], *[pltpu.HBM(s.shape, s.dtype) for s in land_shapes],
                   jax.ShapeDtypeStruct((8, 128), F32)),
        in_specs=[HBM_SPEC] * (n + m) + [ANY_SPEC],
        out_specs=(SEM_SPEC, SEM_SPEC, *[HBM_SPEC] * (n + m), pl.BlockSpec(memory_space=pltpu.VMEM)),
        input_output_aliases={k: 2 + k for k in range(n + m)},
        compiler_params=pltpu.CompilerParams(has_side_effects=DATAFLOW),
    )(*[_in_hbm(s) for s in srcs], *[_in_hbm(l) for l in lands], after)
    return out[0], out[1], list(out[2:2 + n]), list(out[2 + n:2 + n + m]), out[-1]


def _split_copy_wait(send_sems, recv_sems, srcs, lands, copies, after, *, name):
    n, m = len(srcs), len(lands)

    def body(*refs):
        src_refs, land_refs = refs[:n], refs[n:n + m]
        send, recv = refs[n + m], refs[n + m + 1]
        x, y, c = _position()
        for k, (src, dst, peer) in enumerate(copies(x, y, c, src_refs, land_refs)):
            cp = pltpu.make_async_remote_copy(src_ref=src, dst_ref=dst, send_sem=send.at[k], recv_sem=recv.at[k],
                                              device_id=peer, device_id_type=MESH)
            cp.wait_send()
            cp.wait_recv()

    out = _pcall(
        body, name=name,
        out_shape=tuple(pltpu.HBM(s.shape, s.dtype) for s in list(srcs) + list(lands)),
        in_specs=[HBM_SPEC] * (n + m) + [SEM_SPEC, SEM_SPEC, ANY_SPEC], out_specs=tuple([HBM_SPEC] * (n + m)),
        input_output_aliases={k: k for k in range(n + m)},
        compiler_params=pltpu.CompilerParams(has_side_effects=DATAFLOW),
    )(*srcs, *lands, send_sems, recv_sems, after)
    return list(out[:n]), list(out[n:])


def _gather_copies(x, y, c, shard_refs, land_refs, dry=False):
    out = []
    for s, l in zip(shard_refs, land_refs):
        for cx, cy in _other_chips(x, y):
            if dry:
                out.append(None)
                continue
            half = s.shape[0] // 2
            rows = pl.ds(c * half, half)
            out.append((s.at[rows, :], l.at[2 * x + y, rows, :], (cx, cy, c)))
    return out


def _scatter_copies(x, y, c, part_refs, land_refs, dry=False):
    out = []
    for p, l in zip(part_refs, land_refs):
        for r, (cx, cy) in enumerate(_other_chips(x, y)):
            out.append(None if dry else (p.at[2 * cx + cy], l.at[r], (cx, cy, c)))
    return out


def _forward_halves(lands, *, name):
    n = len(lands)

    def body(*refs):
        ins = refs[:n]
        send_sems, recv_sems = refs[2 * n:]
        x, y, c = _position()
        copies = []
        for w in range(n):
            half = ins[w].shape[1] // 2
            for r, (cx, cy) in enumerate(_other_chips(x, y)):
                blk = ins[w].at[2 * cx + cy, pl.ds(c * half, half), :]
                cp = pltpu.make_async_remote_copy(src_ref=blk, dst_ref=blk, send_sem=send_sems.at[w, r],
                                                  recv_sem=recv_sems.at[w, r], device_id=(x, y, 1 - c),
                                                  device_id_type=MESH)
                cp.start()
                copies.append(cp)
        for w in range(n):
            half = ins[w].shape[1] // 2
            for r, (cx, cy) in enumerate(_other_chips(x, y)):
                blk = ins[w].at[2 * cx + cy, pl.ds((1 - c) * half, half), :]
                pltpu.make_async_remote_copy(src_ref=blk, dst_ref=blk, send_sem=send_sems.at[w, r],
                                             recv_sem=recv_sems.at[w, r], device_id=(x, y, 1 - c),
                                             device_id_type=MESH).wait_recv()
        for cp in copies:
            cp.wait_send()

    return _pcall(
        body, name=name, in_specs=[HBM_SPEC] * n, out_specs=[HBM_SPEC] * n,
        out_shape=[jax.ShapeDtypeStruct(l.shape, l.dtype) for l in lands],
        input_output_aliases={k: k for k in range(n)},
        scratch_shapes=[pltpu.SemaphoreType.DMA((n, 3)), pltpu.SemaphoreType.DMA((n, 3))],
    )(*lands)


def _swap_halves(grads, name="swap_halves"):
    n = len(grads)

    def body(*refs):
        ins, outs = refs[:n], refs[n:2 * n]
        send_sems, recv_sems = refs[2 * n:]
        x, y, c = _position()
        copies = []
        for w in range(n):
            half = ins[w].shape[1] // 2
            cp = pltpu.make_async_remote_copy(
                src_ref=ins[w].at[:, pl.ds((1 - c) * half, half), :], dst_ref=outs[w], send_sem=send_sems.at[w],
                recv_sem=recv_sems.at[w], device_id=(x, y, 1 - c), device_id_type=MESH)
            cp.start()
            copies.append(cp)
        for cp in copies:
            cp.wait()

    return _pcall(
        body, name=name, in_specs=[HBM_SPEC] * n, out_specs=[HBM_SPEC] * n,
        out_shape=[jax.ShapeDtypeStruct((4, g.shape[1] // 2, g.shape[2]), g.dtype) for g in grads],
        scratch_shapes=[pltpu.SemaphoreType.DMA((n,)), pltpu.SemaphoreType.DMA((n,))],
    )(*grads)


def _scatter_to_owners(parts):
    n = len(parts)

    def body(*refs):
        ins, outs = refs[:n], refs[n:2 * n]
        send_sems, recv_sems = refs[2 * n:]
        x, y, c = _position()
        copies = []
        for w in range(n):
            for r, (cx, cy) in enumerate(_other_chips(x, y)):
                cp = pltpu.make_async_remote_copy(
                    src_ref=ins[w].at[2 * cx + cy], dst_ref=outs[w].at[r], send_sem=send_sems.at[w, r],
                    recv_sem=recv_sems.at[w, r], device_id=(cx, cy, c), device_id_type=MESH)
                cp.start()
                copies.append(cp)
        for cp in copies:
            cp.wait()

    return _pcall(
        body, name="scatter_to_owners", in_specs=[HBM_SPEC] * n, out_specs=[HBM_SPEC] * n,
        out_shape=[jax.ShapeDtypeStruct((3,) + p.shape[1:], p.dtype) for p in parts],
        scratch_shapes=[pltpu.SemaphoreType.DMA((n, 3)), pltpu.SemaphoreType.DMA((n, 3))],
    )(*parts)


def _share_halves(halves):
    n = len(halves)

    def body(*refs):
        ins, outs = refs[:n], refs[n:2 * n]
        send_sems, recv_sems = refs[2 * n:]
        x, y, c = _position()
        copies = []
        for w in range(n):
            cp = pltpu.make_async_remote_copy(src_ref=ins[w], dst_ref=outs[w], send_sem=send_sems.at[w],
                                              recv_sem=recv_sems.at[w], device_id=(x, y, 1 - c), device_id_type=MESH)
            cp.start()
            copies.append(cp)
        for cp in copies:
            cp.wait()

    return _pcall(
        body, name="share_halves", in_specs=[HBM_SPEC] * n, out_specs=[HBM_SPEC] * n,
        out_shape=[jax.ShapeDtypeStruct(h.shape, h.dtype) for h in halves],
        scratch_shapes=[pltpu.SemaphoreType.DMA((n,)), pltpu.SemaphoreType.DMA((n,))],
    )(*halves)


def _sum_small(part):
    rows, width = part.shape

    def body(x_ref, out_ref, all_ref, send_sems, recv_sems):
        x, y, c = _position()
        me, sibling = (x, y, c), (x, y, 1 - c)
        chips = _other_chips(x, y)

        def block(px, py, pc):
            return all_ref.at[pl.ds((4 * px + 2 * py + pc) * rows, rows), :]

        def copy(k, blk, to, src=None):
            return pltpu.make_async_remote_copy(
                src_ref=block(*blk) if src is None else src, dst_ref=block(*blk), send_sem=send_sems.at[k],
                recv_sem=recv_sems.at[k], device_id=to, device_id_type=MESH)

        all_ref[pl.ds((4 * x + 2 * y + c) * rows, rows), :] = x_ref[...]
        first = [copy(0, me, sibling, src=x_ref)]
        first += [copy(1 + j, me, (*chip, c), src=x_ref) for j, chip in enumerate(chips)]
        for cp in first:
            cp.start()
        passed = [copy(4 + j, (*chip, c), sibling) for j, chip in enumerate(chips)]
        for j, chip in enumerate(chips):
            copy(1 + j, (*chip, c), me).wait_recv()
            passed[j].start()
        copy(0, sibling, me).wait_recv()
        for j, chip in enumerate(chips):
            copy(4 + j, (*chip, 1 - c), me).wait_recv()
        for cp in first + passed:
            cp.wait_send()
        total = all_ref[0:rows, :]
        for d in range(1, 8):
            total = total + all_ref[d * rows:(d + 1) * rows, :]
        out_ref[...] = total

    vm = pl.BlockSpec(memory_space=pltpu.VMEM)
    return _pcall(
        body, name="sum_small", in_specs=[vm], out_specs=vm, out_shape=jax.ShapeDtypeStruct((rows, width), F32),
        scratch_shapes=[pltpu.VMEM((8 * rows, width), F32), pltpu.SemaphoreType.DMA((7,)), pltpu.SemaphoreType.DMA((7,))],
    )(part)


def _row_tile(R, C, itemsize=4, budget=1 << 20):
    for t in (512, 256, 128, 64, 32, 16, 8):
        if R % t == 0 and t * C * itemsize <= budget:
            return t
    return R


def _add_halves(g, recv, c, *, name):
    _, R, C = g.shape
    half = R // 2
    t = _row_tile(half, C)
    nb = half // t

    def body(c_ref, g_ref, r_ref, o_ref):
        o_ref[...] = (g_ref[...].astype(F32) + r_ref[...].astype(F32)).astype(o_ref.dtype)

    grid_spec = pltpu.PrefetchScalarGridSpec(
        num_scalar_prefetch=1, grid=(4, nb),
        in_specs=[pl.BlockSpec((1, t, C), lambda k, i, cr: (k, cr[0] * nb + i, 0)),
                  pl.BlockSpec((1, t, C), lambda k, i, cr: (k, i, 0))],
        out_specs=pl.BlockSpec((1, t, C), lambda k, i, cr: (k, i, 0)))
    return _pcall(body, name=name, grid_spec=grid_spec, out_shape=jax.ShapeDtypeStruct((4, half, C), g.dtype),
                  compiler_params=_params("parallel", "parallel"))(c, g, recv)


def _add_owners(mine, recv, *, name):
    half, C = mine.shape
    t = _row_tile(half, C)

    def body(m_ref, r_ref, o_ref):
        o_ref[...] = ((m_ref[...].astype(F32) + r_ref[0].astype(F32)) + r_ref[1].astype(F32)) + r_ref[2].astype(F32)

    return _pcall(body, name=name, grid=(half // t,),
                  in_specs=[pl.BlockSpec((t, C), lambda i: (i, 0)), pl.BlockSpec((3, t, C), lambda i: (0, i, 0))],
                  out_specs=pl.BlockSpec((t, C), lambda i: (i, 0)), out_shape=jax.ShapeDtypeStruct((half, C), F32),
                  compiler_params=_params("parallel"))(mine, recv)


def _adamw(w, g, m, v, *, name):
    R, C = w.shape
    t = _row_tile(R, C)
    c1 = 1.0 - ADAM_B1 ** ADAM_STEP
    c2 = 1.0 - ADAM_B2 ** ADAM_STEP

    def body(w_ref, g_ref, m_ref, v_ref, d_ref, nm_ref, nv_ref):
        gv = g_ref[...]
        mn = ADAM_B1 * m_ref[...] + (1.0 - ADAM_B1) * gv
        vn = ADAM_B2 * v_ref[...] + (1.0 - ADAM_B2) * (gv * gv)
        d_ref[...] = -ADAM_LR * ((mn / c1) / (jnp.sqrt(vn / c2) + ADAM_EPS) + ADAM_WD * w_ref[...])
        nm_ref[...] = mn
        nv_ref[...] = vn

    blk = pl.BlockSpec((t, C), lambda i: (i, 0))
    shp = jax.ShapeDtypeStruct((R, C), F32)
    return _pcall(body, name=name, grid=(R // t,), in_specs=[blk] * 4, out_specs=[blk] * 3, out_shape=[shp] * 3,
                  compiler_params=_params("parallel"))(w, g, m, v)


BIG = ("w_in", "w_dil_out", "w_fox_out", "w_out", "w_ffn_in", "w_ffn_down")
SMALL = ("norm_mix_g", "b_fgt", "b_gate", "norm_ffn_g", "norm_final_g")
ORDER = ("norm_mix_g", "w_in", "b_fgt", "b_gate", "w_dil_out", "w_fox_out", "w_out", "norm_ffn_g", "w_ffn_in",
         "w_ffn_down", "norm_final_g")
SMALL_ROWS = {"norm_mix_g": (0, 1), "b_gate": (1, 3), "norm_ffn_g": (3, 4), "norm_final_g": (4, 5), "b_fgt": (5, 6)}


def _columns_to_blocks(full, ncol):
    K = full.shape[0]
    return full.reshape(K, 4, ncol).transpose(1, 0, 2)


def _blocks_to_columns(blocks):
    n, K, ncol = blocks.shape
    return blocks.transpose(1, 0, 2).reshape(K, n * ncol)


def kernel(x, norm_mix_g, w_in, b_fgt, b_gate, w_dil_out, w_fox_out, w_out, norm_ffn_g, w_ffn_in, w_ffn_down, norm_final_g, loss_target, m_norm_mix_g, m_w_in, m_b_fgt, m_b_gate, m_w_dil_out, m_w_fox_out, m_w_out, m_norm_ffn_g, m_w_ffn_in, m_w_ffn_down, m_norm_final_g, v_norm_mix_g, v_w_in, v_b_fgt, v_b_gate, v_w_dil_out, v_w_fox_out, v_w_out, v_norm_ffn_g, v_w_ffn_in, v_w_ffn_down, v_norm_final_g):
    weights = dict(norm_mix_g=norm_mix_g, w_in=w_in, b_fgt=b_fgt, b_gate=b_gate, w_dil_out=w_dil_out,
                   w_fox_out=w_fox_out, w_out=w_out, norm_ffn_g=norm_ffn_g, w_ffn_in=w_ffn_in, w_ffn_down=w_ffn_down,
                   norm_final_g=norm_final_g)
    m_in = dict(norm_mix_g=m_norm_mix_g, w_in=m_w_in, b_fgt=m_b_fgt, b_gate=m_b_gate, w_dil_out=m_w_dil_out,
                w_fox_out=m_w_fox_out, w_out=m_w_out, norm_ffn_g=m_norm_ffn_g, w_ffn_in=m_w_ffn_in,
                w_ffn_down=m_w_ffn_down, norm_final_g=m_norm_final_g)
    v_in = dict(norm_mix_g=v_norm_mix_g, w_in=v_w_in, b_fgt=v_b_fgt, b_gate=v_b_gate, w_dil_out=v_w_dil_out,
                w_fox_out=v_w_fox_out, w_out=v_w_out, norm_ffn_g=v_norm_ffn_g, w_ffn_in=v_w_ffn_in,
                w_ffn_down=v_w_ffn_down, norm_final_g=v_norm_final_g)
    c = lax.axis_index("c")
    chip = 2 * lax.axis_index("x") + lax.axis_index("y")

    shards = {n: weights[n][0].astype(_CD) for n in BIG}
    (g_in,) = _gather_weights([shards["w_in"]])
    late = BIG[1:]
    send_g, recv_g, late_src, late_land, token = _split_copy_start(
        [shards[n] for n in late], [jax.ShapeDtypeStruct((4,) + shards[n].shape, _CD) for n in late],
        _gather_copies, g_in, name="gather_late_start")
    full_in = _blocks_to_columns(lax.dynamic_update_index_in_dim(g_in, shards["w_in"], chip, 0))
    o3 = QKV_COLS
    o4 = o3 + N_FOX_HEADS
    w = dict(qkv=full_in[:, :o3], f=jnp.pad(full_in[:, o3:o4], ((0, 0), (0, F_PAD - N_FOX_HEADS))), g=full_in[:, o4:])
    p = dict(norm_mix_g=norm_mix_g, b_fgt=jnp.pad(b_fgt, ((0, 0), (0, F_PAD - N_FOX_HEADS))), b_gate=b_gate,
             norm_ffn_g=norm_ffn_g, norm_final_g=norm_final_g.reshape(1, D_MODEL))

    def late_weights(after):
        own, lands = _split_copy_wait(send_g, recv_g, late_src, late_land, _gather_copies, after,
                                      name="gather_late_wait")
        lands = _forward_halves(lands, name="gather_late_forward")
        g_dil, g_fox, g_out, g_ffn_in, g_ffn_down = [
            lax.dynamic_update_index_in_dim(l, s, chip, 0) for l, s in zip(lands, own)]
        return dict(dil_out=_blocks_to_columns(g_dil), fox_out=_blocks_to_columns(g_fox),
                    out=g_out.reshape(D_MODEL, D_MODEL), ffn_in=_blocks_to_columns(g_ffn_in),
                    ffn_down=g_ffn_down.reshape(D_FF, D_MODEL))

    c_arr = jnp.reshape(c, (1,)).astype(jnp.int32)

    def to_blocks(n, full):
        shape = weights[n].shape
        if n in ("w_out", "w_ffn_down"):
            return full.reshape(4, shape[1], shape[2])
        return _columns_to_blocks(full, shape[2])

    def pair_sums(group, named):
        names = list(named)
        blocks = [to_blocks(n, named[n]) for n in names]
        from_sibling = _swap_halves(blocks, name=f"swap_halves_{group}")
        return [_add_halves(b, r, c_arr, name=f"add_halves_{n}") for b, r, n in zip(blocks, from_sibling, names)]

    in_flight = {}

    def grad_sink(group, gw):
        named = {"w_" + k: v for k, v in gw.items()}
        sums = pair_sums(group, named)
        started = _split_copy_start(sums, [jax.ShapeDtypeStruct((3,) + s.shape[1:], s.dtype) for s in sums],
                                    _scatter_copies, next(iter(gw.values())), name=f"scatter_{group}_start")
        in_flight[group] = (list(named), started)

    loss_part, grad_x, gw, small = _layer_step(x[0], loss_target[0], w, p, late_weights, grad_sink, token)

    def owner_sums(names, sums, from_chips):
        return {n: _add_owners(lax.dynamic_index_in_dim(s, chip, 0, keepdims=False), r, name=f"add_owners_{n}")
                for n, s, r in zip(names, sums, from_chips)}

    gw_in = jnp.concatenate([gw["qkv"], gw["f"][:, :N_FOX_HEADS], gw["g"]], axis=1)
    sums_in = pair_sums("in", {"w_in": gw_in})
    halves = owner_sums(["w_in"], sums_in, _scatter_to_owners(sums_in))
    for group, (names, (send_s, recv_s, srcs, lands, _)) in in_flight.items():
        sums, from_chips = _split_copy_wait(send_s, recv_s, srcs, lands, _scatter_copies, halves["w_in"],
                                            name=f"scatter_{group}_wait")
        halves.update(owner_sums(names, sums, from_chips))
    halves = [halves[n] for n in BIG]
    grads = {}
    for n, own, other in zip(BIG, halves, _share_halves(halves)):
        pair = jnp.stack([own, other])
        grads[n] = jnp.where(c == 0, pair, pair[::-1]).reshape(2 * own.shape[0], own.shape[1])

    packed = jnp.concatenate([
        small["norm_mix_g"], small["b_gate"].reshape(2, D_MODEL), small["norm_ffn_g"], small["norm_final_g"],
        jnp.pad(small["b_fgt"], ((0, 0), (0, D_MODEL - F_PAD))), jnp.zeros((2, D_MODEL), F32)], axis=0)
    summed = _sum_small(packed)
    for n in SMALL:
        lo, hi = SMALL_ROWS[n]
        grads[n] = summed[lo:hi].reshape(1, -1)[:, :weights[n].size]

    loss = lax.psum(loss_part[0, 0], ("x", "y", "c"))

    out_g, out_d, out_m, out_v = {}, {}, {}, {}
    for n in ORDER:
        shape = weights[n].shape
        two_d = shape[1:] if len(shape) == 3 else (1, weights[n].size)
        g2 = grads[n].reshape(two_d)
        d2, m2, v2 = _adamw(weights[n].reshape(two_d), g2, m_in[n].reshape(two_d), v_in[n].reshape(two_d),
                            name=f"adamw_{n}")
        out_g[n], out_d[n], out_m[n], out_v[n] = (g2.reshape(shape), d2.reshape(shape), m2.reshape(shape),
                                                  v2.reshape(shape))
    return (loss, grad_x[None], *[out_g[n] for n in ORDER], *[out_d[n] for n in ORDER],
            *[out_m[n] for n in ORDER], *[out_v[n] for n in ORDER])
```

```python
import numpy as np
import jax
import jax.numpy as jnp
from jax import lax
from jax.experimental import pallas as pl
from jax.experimental.pallas import tpu as pltpu

F32 = jnp.float32
_CD = jnp.bfloat16

D_MODEL = 1024
HEAD_DIM = 64
DIL_PAIRS = ((128, 1), (512, 4), (2048, 16))
N_DIL_GROUPS = 3
DIL_HEADS = 4
DIL_W = 128
DIL_OUT = DIL_HEADS * HEAD_DIM
DIL_WIDTH = N_DIL_GROUPS * DIL_OUT
N_FOX_HEADS = 8
FOX_WIDTH = N_FOX_HEADS * HEAD_DIM
D_FF = 2816
QKV_COLS = 3 * DIL_WIDTH + 3 * FOX_WIDTH
F_PAD = 128
RMS_EPS = 1e-6
NEG_INF = -1e30
ATTN_SCALE = HEAD_DIM ** -0.5
ADAM_LR, ADAM_B1, ADAM_B2, ADAM_EPS, ADAM_WD, ADAM_STEP = 0.001, 0.9, 0.999, 1e-08, 0.01, 10

VMEM_LIMIT = 48 * 1024 * 1024
MESH = pl.DeviceIdType.MESH
HBM_SPEC = pl.BlockSpec(memory_space=pltpu.HBM)


def _pcall(body, **kw):
    return pl.pallas_call(body, **kw)


def _params(*sem):
    return pltpu.CompilerParams(dimension_semantics=sem, vmem_limit_bytes=VMEM_LIMIT)


def _pick(dim, pref):
    t = (min(pref, dim) // 128) * 128
    while t >= 128:
        if dim % t == 0:
            return t
        t -= 128
    return dim


def _mm(a, b, *, name, ta=False, tb=False, out_dtype=F32, add=None, tm=1024, tn=512, tk=2048):
    if ta:
        K, M = a.shape
    else:
        M, K = a.shape
    if tb:
        N, K2 = b.shape
    else:
        K2, N = b.shape
    assert K == K2, (a.shape, b.shape)
    tm, tn, tk = _pick(M, tm), _pick(N, tn), _pick(K, tk)
    nk = K // tk
    dn = (((0 if ta else 1,), (1 if tb else 0,)), ((), ()))
    has_add = add is not None

    def body(*refs):
        a_ref, b_ref = refs[0], refs[1]
        add_ref = refs[2] if has_add else None
        o_ref = refs[3] if has_add else refs[2]
        p = lax.dot_general(a_ref[...].astype(_CD), b_ref[...].astype(_CD), dn, preferred_element_type=F32)

        def finish(r):
            if has_add:
                r = r + add_ref[...]
            o_ref[...] = r.astype(out_dtype)

        if nk == 1:
            finish(p)
        else:
            acc_ref = refs[-1]
            k = pl.program_id(2)

            @pl.when(k == 0)
            def _():
                acc_ref[...] = p

            @pl.when(k > 0)
            def _():
                acc_ref[...] += p

            @pl.when(k == nk - 1)
            def _():
                finish(acc_ref[...])

    a_spec = pl.BlockSpec((tk, tm), lambda i, j, k: (k, i)) if ta else pl.BlockSpec((tm, tk), lambda i, j, k: (i, k))
    b_spec = pl.BlockSpec((tn, tk), lambda i, j, k: (j, k)) if tb else pl.BlockSpec((tk, tn), lambda i, j, k: (k, j))
    o_spec = pl.BlockSpec((tm, tn), lambda i, j, k: (i, j))
    in_specs = [a_spec, b_spec] + ([o_spec] if has_add else [])
    args = (a, b) + ((add,) if has_add else ())
    return _pcall(
        body, name=name, grid=(M // tm, N // tn, nk), in_specs=in_specs, out_specs=o_spec,
        out_shape=jax.ShapeDtypeStruct((M, N), out_dtype),
        scratch_shapes=[pltpu.VMEM((tm, tn), F32)] if nk > 1 else [],
        compiler_params=_params("parallel", "parallel", "arbitrary"),
    )(*args)


def _rms_fwd(x, g, *, name, tm=512, after=None):
    S, D = x.shape

    def body(x_ref, g_ref, *rest):
        h_ref = rest[-1]
        xv = x_ref[...]
        r = lax.rsqrt(jnp.mean(xv * xv, axis=-1, keepdims=True) + RMS_EPS)
        h_ref[...] = ((xv * r) * g_ref[...]).astype(h_ref.dtype)

    row = pl.BlockSpec((tm, D), lambda i: (i, 0))
    extra = [] if after is None else [after]
    return _pcall(body, name=name, grid=(S // tm,),
                  in_specs=[row, pl.BlockSpec((1, D), lambda i: (0, 0))] + [pl.BlockSpec(memory_space=pl.ANY)] * len(extra),
                  out_specs=row, out_shape=jax.ShapeDtypeStruct((S, D), _CD),
                  compiler_params=_params("parallel"))(x, g, *extra)


def _rms_bwd(x, g, dh, dres, *, name, tm=512):
    S, D = x.shape

    def body(x_ref, g_ref, dh_ref, dres_ref, dx_ref, dg_ref):
        xv = x_ref[...]
        r = lax.rsqrt(jnp.mean(xv * xv, axis=-1, keepdims=True) + RMS_EPS)
        xh = xv * r
        dhv = dh_ref[...]
        dxh = dhv * g_ref[...]
        dx_ref[...] = dres_ref[...] + r * (dxh - xh * jnp.mean(dxh * xh, axis=-1, keepdims=True))
        part = jnp.sum(dhv * xh, axis=0, keepdims=True)

        @pl.when(pl.program_id(0) == 0)
        def _():
            dg_ref[...] = part

        @pl.when(pl.program_id(0) > 0)
        def _():
            dg_ref[...] += part

    row = pl.BlockSpec((tm, D), lambda i: (i, 0))
    vec = pl.BlockSpec((1, D), lambda i: (0, 0))
    return _pcall(body, name=name, grid=(S // tm,), in_specs=[row, vec, row, row], out_specs=[row, vec],
                  out_shape=[jax.ShapeDtypeStruct((S, D), F32), jax.ShapeDtypeStruct((1, D), F32)],
                  compiler_params=_params("arbitrary"))(x, g, dh, dres)


def _loss_head(x, g, tgt, *, name, tm=512):
    S, D = x.shape

    def body(x_ref, g_ref, t_ref, loss_ref, dx_ref, dg_ref):
        xv = x_ref[...]
        gv = g_ref[...]
        r = lax.rsqrt(jnp.mean(xv * xv, axis=-1, keepdims=True) + RMS_EPS)
        xh = xv * r
        err = xh * gv - t_ref[...]
        lpart = 0.5 * jnp.sum(jnp.mean(err * err, axis=-1, keepdims=True), axis=0, keepdims=True)
        dy = err * (1.0 / D)
        dxh = dy * gv
        dx_ref[...] = r * (dxh - xh * jnp.mean(dxh * xh, axis=-1, keepdims=True))
        gpart = jnp.sum(dy * xh, axis=0, keepdims=True)

        @pl.when(pl.program_id(0) == 0)
        def _():
            loss_ref[...] = lpart
            dg_ref[...] = gpart

        @pl.when(pl.program_id(0) > 0)
        def _():
            loss_ref[...] += lpart
            dg_ref[...] += gpart

    row = pl.BlockSpec((tm, D), lambda i: (i, 0))
    vec = pl.BlockSpec((1, D), lambda i: (0, 0))
    one = pl.BlockSpec((1, 1), lambda i: (0, 0))
    return _pcall(body, name=name, grid=(S // tm,), in_specs=[row, vec, row], out_specs=[one, row, vec],
                  out_shape=[jax.ShapeDtypeStruct((1, 1), F32), jax.ShapeDtypeStruct((S, D), F32),
                             jax.ShapeDtypeStruct((1, D), F32)],
                  compiler_params=_params("arbitrary"))(x, g, tgt)


def _sigmoid(z):
    return 1.0 / (1.0 + jnp.exp(-z))


def _gate_fwd(gl, bg, ya, yb, *, name, tm=512):
    S, D = ya.shape

    def body(za_ref, zb_ref, ba_ref, bb_ref, ya_ref, yb_ref, o_ref):
        ga = _sigmoid(za_ref[...] + ba_ref[...])
        gb = _sigmoid(zb_ref[...] + bb_ref[...])
        o_ref[...] = (ga * ya_ref[...] + gb * yb_ref[...]).astype(o_ref.dtype)

    lo = pl.BlockSpec((tm, D), lambda i: (i, 0))
    hi = pl.BlockSpec((tm, D), lambda i: (i, 1))
    vlo = pl.BlockSpec((1, D), lambda i: (0, 0))
    vhi = pl.BlockSpec((1, D), lambda i: (0, 1))
    return _pcall(body, name=name, grid=(S // tm,), in_specs=[lo, hi, vlo, vhi, lo, lo], out_specs=lo,
                  out_shape=jax.ShapeDtypeStruct((S, D), _CD), compiler_params=_params("parallel"))(gl, gl, bg, bg, ya, yb)


def _gate_bwd(dm, gl, bg, ya, yb, *, name, tm=256):
    S, D = ya.shape

    def body(dm_ref, za_ref, zb_ref, ba_ref, bb_ref, ya_ref, yb_ref, dya_ref, dyb_ref, dgl_ref, dbg_ref):
        dmv = dm_ref[...]
        ga = _sigmoid(za_ref[...] + ba_ref[...])
        gb = _sigmoid(zb_ref[...] + bb_ref[...])
        dya_ref[...] = (dmv * ga).astype(dya_ref.dtype)
        dyb_ref[...] = (dmv * gb).astype(dyb_ref.dtype)
        dza = dmv * ya_ref[...] * ga * (1.0 - ga)
        dzb = dmv * yb_ref[...] * gb * (1.0 - gb)
        dgl_ref[:, :D] = dza.astype(dgl_ref.dtype)
        dgl_ref[:, D:] = dzb.astype(dgl_ref.dtype)
        pa = jnp.sum(dza, axis=0, keepdims=True)
        pb = jnp.sum(dzb, axis=0, keepdims=True)

        @pl.when(pl.program_id(0) == 0)
        def _():
            dbg_ref[:, :D] = pa
            dbg_ref[:, D:] = pb

        @pl.when(pl.program_id(0) > 0)
        def _():
            dbg_ref[:, :D] += pa
            dbg_ref[:, D:] += pb

    lo = pl.BlockSpec((tm, D), lambda i: (i, 0))
    hi = pl.BlockSpec((tm, D), lambda i: (i, 1))
    vlo = pl.BlockSpec((1, D), lambda i: (0, 0))
    vhi = pl.BlockSpec((1, D), lambda i: (0, 1))
    wide = pl.BlockSpec((tm, 2 * D), lambda i: (i, 0))
    vwide = pl.BlockSpec((1, 2 * D), lambda i: (0, 0))
    return _pcall(body, name=name, grid=(S // tm,), in_specs=[lo, lo, hi, vlo, vhi, lo, lo],
                  out_specs=[lo, lo, wide, vwide],
                  out_shape=[jax.ShapeDtypeStruct((S, D), _CD), jax.ShapeDtypeStruct((S, D), _CD),
                             jax.ShapeDtypeStruct((S, 2 * D), _CD), jax.ShapeDtypeStruct((1, 2 * D), F32)],
                  compiler_params=_params("arbitrary"))(dm, gl, gl, bg, bg, ya, yb)


def _swiglu_fwd(gu, *, name, tm=256):
    S, F2 = gu.shape
    F = F2 // 2

    def body(g_ref, u_ref, o_ref):
        gv = g_ref[...]
        o_ref[...] = (gv * _sigmoid(gv) * u_ref[...]).astype(o_ref.dtype)

    lo = pl.BlockSpec((tm, F), lambda i: (i, 0))
    hi = pl.BlockSpec((tm, F), lambda i: (i, 1))
    return _pcall(body, name=name, grid=(S // tm,), in_specs=[lo, hi], out_specs=lo,
                  out_shape=jax.ShapeDtypeStruct((S, F), _CD), compiler_params=_params("parallel"))(gu, gu)


def _swiglu_bwd(dact, gu, *, name, tm=256):
    S, F2 = gu.shape
    F = F2 // 2

    def body(d_ref, g_ref, u_ref, o_ref):
        dv = d_ref[...]
        gv = g_ref[...]
        sg = _sigmoid(gv)
        o_ref[:, :F] = (dv * u_ref[...] * (sg * (1.0 + gv * (1.0 - sg)))).astype(o_ref.dtype)
        o_ref[:, F:] = (dv * (gv * sg)).astype(o_ref.dtype)

    lo = pl.BlockSpec((tm, F), lambda i: (i, 0))
    hi = pl.BlockSpec((tm, F), lambda i: (i, 1))
    return _pcall(body, name=name, grid=(S // tm,), in_specs=[lo, lo, hi],
                  out_specs=pl.BlockSpec((tm, F2), lambda i: (i, 0)),
                  out_shape=jax.ShapeDtypeStruct((S, F2), _CD), compiler_params=_params("parallel"))(dact, gu, gu)


def _split3(x):
    hi = x.astype(jnp.bfloat16)
    r1 = x - hi.astype(F32)
    mid = r1.astype(jnp.bfloat16)
    lo = (r1 - mid.astype(F32)).astype(jnp.bfloat16)
    return hi, mid, lo


def _ones_dot_left(ones, x):
    return sum(jnp.dot(ones, p, preferred_element_type=F32) for p in _split3(x))


def _ones_dot_right(x, ones):
    return sum(jnp.dot(p, ones, preferred_element_type=F32) for p in _split3(x))


def _head_sum(x):
    n = x.shape[1]
    r = lax.broadcasted_iota(jnp.int32, (n, n), 0) // HEAD_DIM
    c = lax.broadcasted_iota(jnp.int32, (n, n), 1) // HEAD_DIM
    return _ones_dot_right(x, (r == c).astype(jnp.bfloat16))


def _log_sigmoid(z):
    e = jnp.exp(-jnp.abs(z))
    t = 1.0 + e
    log1p_e = jnp.where(t == 1.0, e, jnp.log(t) * (e / jnp.where(t == 1.0, 1.0, t - 1.0)))
    return jnp.minimum(z, 0.0) - log1p_e


def _fox_cumsum(zf, bf, *, name):
    S, W = zf.shape
    nb = S // 128

    def body(z_ref, b_ref, c_ref):
        tri = (lax.broadcasted_iota(jnp.int32, (128, 128), 0) >= lax.broadcasted_iota(jnp.int32, (128, 128), 1))
        tri = tri.astype(jnp.bfloat16)

        def step(i, carry):
            rows = pl.ds(pl.multiple_of(i * 128, 128), 128)
            lf = _log_sigmoid(z_ref[rows, :] + b_ref[...])
            cb = _ones_dot_left(tri, lf) + carry
            c_ref[rows, :] = cb
            return cb[127:128, :]

        lax.fori_loop(0, nb, step, jnp.zeros((1, W), F32))

    return _pcall(body, name=name, out_shape=jax.ShapeDtypeStruct((S, W), F32),
                  compiler_params=pltpu.CompilerParams(vmem_limit_bytes=VMEM_LIMIT))(zf, bf)


def _fox_cumsum_bwd(dc, zf, bf, *, name):
    S, W = zf.shape
    nb = S // 128

    def body(dc_ref, z_ref, b_ref, dz_ref, db_ref):
        tri = (lax.broadcasted_iota(jnp.int32, (128, 128), 0) <= lax.broadcasted_iota(jnp.int32, (128, 128), 1))
        tri = tri.astype(jnp.bfloat16)

        def step(k, carry):
            tail, acc = carry
            i = nb - 1 - k
            rows = pl.ds(pl.multiple_of(i * 128, 128), 128)
            dlf = _ones_dot_left(tri, dc_ref[rows, :]) + tail
            dz = dlf * _sigmoid(-(z_ref[rows, :] + b_ref[...]))
            dz_ref[rows, :] = dz
            return dlf[0:1, :], acc + jnp.sum(dz, axis=0, keepdims=True)

        _, acc = lax.fori_loop(0, nb, step, (jnp.zeros((1, W), F32), jnp.zeros((1, W), F32)))
        db_ref[...] = acc

    return _pcall(body, name=name,
                  out_shape=[jax.ShapeDtypeStruct((S, W), F32), jax.ShapeDtypeStruct((1, W), F32)],
                  compiler_params=pltpu.CompilerParams(vmem_limit_bytes=VMEM_LIMIT))(dc, zf, bf)


def _dil_slopes(group):
    h = np.arange(1, N_DIL_GROUPS * DIL_HEADS + 1, dtype=np.float32)
    s = (np.float32(2.0) ** (np.float32(-8.0) * h / np.float32(N_DIL_GROUPS * DIL_HEADS))).astype(np.float32)
    return [float(v) for v in s.reshape(N_DIL_GROUPS, DIL_HEADS)[group]]


def _dil_tiles(i, n, blocks_per_seq):
    qi = lax.broadcasted_iota(jnp.int32, (DIL_W, DIL_W), 0)
    kj = lax.broadcasted_iota(jnp.int32, (DIL_W, DIL_W), 1)
    first = ((4 * n + i) % blocks_per_seq) == 0
    valid_prev = jnp.logical_and(kj >= qi, jnp.logical_not(first))
    valid_cur = kj <= qi
    rel_prev = (qi - kj + DIL_W).astype(F32)
    rel_cur = (qi - kj).astype(F32)
    return valid_prev, valid_cur, rel_prev, rel_cur


CHUNK = 4 * DIL_W


def _dil_fwd(q, k, v, group, *, name):
    S = q.shape[0]
    dilation = DIL_PAIRS[group][1]
    bps = (S // dilation) // DIL_W
    slopes = _dil_slopes(group)
    nt = (((1,), (1,)), ((), ()))

    def body(q_ref, k_ref, v_ref, kp_ref, vp_ref, o_ref, l_ref):
        n = pl.program_id(0)
        for i in range(4):
            valid_prev, valid_cur, rel_prev, rel_cur = _dil_tiles(i, n, bps)
            rows = slice(i * DIL_W, (i + 1) * DIL_W)
            prow = slice((i - 1) * DIL_W, i * DIL_W)
            for h in range(DIL_HEADS):
                cols = slice(h * HEAD_DIM, (h + 1) * HEAD_DIM)
                qh = q_ref[rows, cols]
                kc, vc = k_ref[rows, cols], v_ref[rows, cols]
                kp = kp_ref[:, cols] if i == 0 else k_ref[prow, cols]
                vp = vp_ref[:, cols] if i == 0 else v_ref[prow, cols]
                sl = slopes[h] * dilation
                sp = lax.dot_general(qh, kp, nt, preferred_element_type=F32) * ATTN_SCALE - sl * rel_prev
                sc = lax.dot_general(qh, kc, nt, preferred_element_type=F32) * ATTN_SCALE - sl * rel_cur
                sp = jnp.where(valid_prev, sp, NEG_INF)
                sc = jnp.where(valid_cur, sc, NEG_INF)
                m = jnp.maximum(jnp.max(sp, axis=-1, keepdims=True), jnp.max(sc, axis=-1, keepdims=True))
                pp, pc = jnp.exp(sp - m), jnp.exp(sc - m)
                den = jnp.sum(pp, axis=-1, keepdims=True) + jnp.sum(pc, axis=-1, keepdims=True)
                acc = (jnp.dot(pp.astype(_CD), vp, preferred_element_type=F32)
                       + jnp.dot(pc.astype(_CD), vc, preferred_element_type=F32))
                o_ref[rows, cols] = acc / den
                l_ref[rows, cols] = jnp.broadcast_to(m + jnp.log(den), (DIL_W, HEAD_DIM))

    cur = pl.BlockSpec((CHUNK, DIL_OUT), lambda n: (n, 0))
    prev = pl.BlockSpec((DIL_W, DIL_OUT), lambda n: (jnp.maximum(4 * n - 1, 0), 0))
    return _pcall(body, name=name, grid=(S // CHUNK,), in_specs=[cur, cur, cur, prev, prev], out_specs=[cur, cur],
                  out_shape=[jax.ShapeDtypeStruct((S, DIL_OUT), F32), jax.ShapeDtypeStruct((S, DIL_OUT), F32)],
                  compiler_params=_params("parallel"))(q, k, v, k, v)


def _dil_bwd(q, k, v, o, lse, do, dlse, group, *, name):
    S = q.shape[0]
    dilation = DIL_PAIRS[group][1]
    bps = (S // dilation) // DIL_W
    slopes = _dil_slopes(group)
    nchunk = S // CHUNK
    nt = (((1,), (1,)), ((), ()))
    tn = (((0,), (0,)), ((), ()))

    def body(q_ref, k_ref, v_ref, kp_ref, vp_ref, o_ref, l_ref, do_ref, dl_ref, dq_ref, dk_ref, dv_ref,
             dk_s, dv_s):
        step = pl.program_id(0)
        n = nchunk - 1 - step

        @pl.when(step == 0)
        def _():
            dk_s[CHUNK:, :] = jnp.zeros((DIL_W, DIL_OUT), F32)
            dv_s[CHUNK:, :] = jnp.zeros((DIL_W, DIL_OUT), F32)

        dk_s[:CHUNK, :] = jnp.zeros((CHUNK, DIL_OUT), F32)
        dv_s[:CHUNK, :] = jnp.zeros((CHUNK, DIL_OUT), F32)
        for i in range(4):
            valid_prev, valid_cur, rel_prev, rel_cur = _dil_tiles(i, n, bps)
            rows = slice(i * DIL_W, (i + 1) * DIL_W)
            prow = slice((i - 1) * DIL_W, i * DIL_W)
            s_prev = slice(i * DIL_W, (i + 1) * DIL_W)
            s_cur = slice((i + 1) * DIL_W, (i + 2) * DIL_W)
            for h in range(DIL_HEADS):
                cols = slice(h * HEAD_DIM, (h + 1) * HEAD_DIM)
                qh = q_ref[rows, cols]
                kc, vc = k_ref[rows, cols], v_ref[rows, cols]
                kp = kp_ref[:, cols] if i == 0 else k_ref[prow, cols]
                vp = vp_ref[:, cols] if i == 0 else v_ref[prow, cols]
                sl = slopes[h] * dilation
                lh = l_ref[rows, h * HEAD_DIM:h * HEAD_DIM + 1]
                sp = lax.dot_general(qh, kp, nt, preferred_element_type=F32) * ATTN_SCALE - sl * rel_prev
                sc = lax.dot_general(qh, kc, nt, preferred_element_type=F32) * ATTN_SCALE - sl * rel_cur
                pp = jnp.exp(jnp.where(valid_prev, sp, NEG_INF) - lh)
                pc = jnp.exp(jnp.where(valid_cur, sc, NEG_INF) - lh)
                doh = do_ref[rows, cols]
                dsum = jnp.sum(doh * o_ref[rows, cols], axis=-1, keepdims=True)
                shift = dl_ref[rows, h * HEAD_DIM:h * HEAD_DIM + 1] - dsum
                dob = doh.astype(_CD)
                dsp = pp * (lax.dot_general(dob, vp, nt, preferred_element_type=F32) + shift)
                dsc = pc * (lax.dot_general(dob, vc, nt, preferred_element_type=F32) + shift)
                dspb = (dsp * ATTN_SCALE).astype(_CD)
                dscb = (dsc * ATTN_SCALE).astype(_CD)
                dq_ref[rows, cols] = (jnp.dot(dspb, kp, preferred_element_type=F32)
                                      + jnp.dot(dscb, kc, preferred_element_type=F32)).astype(dq_ref.dtype)
                dk_s[s_prev, cols] += lax.dot_general(dspb, qh, tn, preferred_element_type=F32)
                dk_s[s_cur, cols] += lax.dot_general(dscb, qh, tn, preferred_element_type=F32)
                dv_s[s_prev, cols] += lax.dot_general(pp.astype(_CD), dob, tn, preferred_element_type=F32)
                dv_s[s_cur, cols] += lax.dot_general(pc.astype(_CD), dob, tn, preferred_element_type=F32)
        dk_ref[...] = dk_s[DIL_W:, :].astype(dk_ref.dtype)
        dv_ref[...] = dv_s[DIL_W:, :].astype(dv_ref.dtype)
        dk_s[CHUNK:, :] = dk_s[:DIL_W, :]
        dv_s[CHUNK:, :] = dv_s[:DIL_W, :]

    cur = pl.BlockSpec((CHUNK, DIL_OUT), lambda s: (nchunk - 1 - s, 0))
    prev = pl.BlockSpec((DIL_W, DIL_OUT), lambda s: (jnp.maximum(4 * (nchunk - 1 - s) - 1, 0), 0))
    shp = jax.ShapeDtypeStruct((S, DIL_OUT), _CD)
    return _pcall(body, name=name, grid=(nchunk,), in_specs=[cur, cur, cur, prev, prev, cur, cur, cur, cur],
                  out_specs=[cur, cur, cur], out_shape=[shp, shp, shp],
                  scratch_shapes=[pltpu.VMEM((CHUNK + DIL_W, DIL_OUT), F32), pltpu.VMEM((CHUNK + DIL_W, DIL_OUT), F32)],
                  compiler_params=_params("arbitrary"))(q, k, v, k, v, o, lse, do, dlse)


def _dil_mix_fwd(os_, ls_, *, name, tm=512):
    S, W = os_[0].shape

    def body(o0, o1, o2, l0, l1, l2, out_ref):
        ls = [l0[...], l1[...], l2[...]]
        m = jnp.maximum(jnp.maximum(ls[0], ls[1]), ls[2])
        es = [jnp.exp(l - m) for l in ls]
        den = es[0] + es[1] + es[2]
        out_ref[...] = ((es[0] * o0[...] + es[1] * o1[...] + es[2] * o2[...]) / den).astype(out_ref.dtype)

    row = pl.BlockSpec((tm, W), lambda i: (i, 0))
    return _pcall(body, name=name, grid=(S // tm,), in_specs=[row] * 6, out_specs=row,
                  out_shape=jax.ShapeDtypeStruct((S, W), _CD), compiler_params=_params("parallel"))(*os_, *ls_)


def _dil_mix_bwd(doa, os_, ls_, *, name, tm=512):
    S, W = doa.shape

    def body(d_ref, o0, o1, o2, l0, l1, l2, do0, do1, do2, dl0, dl1, dl2):
        dv = d_ref[...]
        ls = [l0[...], l1[...], l2[...]]
        m = jnp.maximum(jnp.maximum(ls[0], ls[1]), ls[2])
        es = [jnp.exp(l - m) for l in ls]
        den = es[0] + es[1] + es[2]
        al = [e / den for e in es]
        da = [_head_sum(dv * o[...]) for o in (o0, o1, o2)]
        mean = al[0] * da[0] + al[1] * da[1] + al[2] * da[2]
        for a, d_, do_ref, dl_ref in zip(al, da, (do0, do1, do2), (dl0, dl1, dl2)):
            do_ref[...] = a * dv
            dl_ref[...] = a * (d_ - mean)

    row = pl.BlockSpec((tm, W), lambda i: (i, 0))
    shp = jax.ShapeDtypeStruct((S, W), F32)
    return _pcall(body, name=name, grid=(S // tm,), in_specs=[row] * 7, out_specs=[row] * 6, out_shape=[shp] * 6,
                  compiler_params=_params("parallel"))(doa, *os_, *ls_)


FOX_T = 512


PACK = 2 * HEAD_DIM
HEAD_PAIRS = N_FOX_HEADS // 2
Q_BLOCK0 = (3 * DIL_WIDTH) // PACK
K_BLOCK0 = (3 * DIL_WIDTH + FOX_WIDTH) // PACK
V_BLOCK0 = (3 * DIL_WIDTH + 2 * FOX_WIDTH) // PACK


def _pieces(x):
    hi = x.astype(jnp.bfloat16).astype(F32)
    r = x - hi
    mid = r.astype(jnp.bfloat16).astype(F32)
    lo = (r - mid).astype(jnp.bfloat16).astype(F32)
    return [hi, mid, lo]


def _extras(first, second, rows):
    lane = lax.broadcasted_iota(jnp.int32, (rows, HEAD_DIM), 1)
    out = jnp.zeros((rows, HEAD_DIM), F32)
    for idx, val in enumerate(list(first) + list(second)):
        out = jnp.where(lane == idx, val, out)
    return out


def _head_column(c, h):
    lane = lax.broadcasted_iota(jnp.int32, c.shape, 1)
    return jnp.sum(jnp.where(lane == h, c, 0.0), axis=1, keepdims=True)


ONES3 = [1.0, 1.0, 1.0]
ZEROS3 = [0.0, 0.0, 0.0]


def _fox_pack_fwd(qkv, c, *, name, tm=512):
    S = qkv.shape[0]

    def body(q_ref, k_ref, v_ref, c_ref, qo_ref, ko_ref, vo_ref):
        hp = pl.program_id(1)
        cv = c_ref[...]
        for hh in range(2):
            ch = _pieces(_head_column(cv, 2 * hp + hh))
            src = slice(hh * HEAD_DIM, (hh + 1) * HEAD_DIM)
            lo = slice(hh * PACK, hh * PACK + HEAD_DIM)
            hi = slice(hh * PACK + HEAD_DIM, (hh + 1) * PACK)
            qo_ref[:, lo] = (q_ref[:, src].astype(F32) * ATTN_SCALE).astype(qo_ref.dtype)
            qo_ref[:, hi] = _extras(ch, ONES3, tm).astype(qo_ref.dtype)
            ko_ref[:, lo] = k_ref[:, src]
            ko_ref[:, hi] = _extras(ONES3, [-p for p in ch], tm).astype(ko_ref.dtype)
            vo_ref[:, lo] = v_ref[:, src]
            vo_ref[:, hi] = _extras(ONES3, ZEROS3, tm).astype(vo_ref.dtype)

    def src(block0):
        return pl.BlockSpec((tm, PACK), lambda i, hp: (i, block0 + hp))

    out = pl.BlockSpec((tm, 2 * PACK), lambda i, hp: (i, hp))
    shp = jax.ShapeDtypeStruct((S, N_FOX_HEADS * PACK), _CD)
    return _pcall(body, name=name, grid=(S // tm, HEAD_PAIRS),
                  in_specs=[src(Q_BLOCK0), src(K_BLOCK0), src(V_BLOCK0), pl.BlockSpec((tm, PACK), lambda i, hp: (i, 0))],
                  out_specs=[out, out, out], out_shape=[shp, shp, shp],
                  compiler_params=_params("parallel", "parallel"))(qkv, qkv, qkv, c)


def _fox_fwd(qp, kp, vp, *, name):
    S = qp.shape[0]
    nt = S // FOX_T
    nt_dims = (((1,), (1,)), ((), ()))
    tn_dims = (((0,), (0,)), ((), ()))

    def body(q_ref, k_ref, v_ref, o_ref, l_ref, m_s, acc_s):
        i, j = pl.program_id(1), pl.program_id(2)

        @pl.when(j == 0)
        def _():
            m_s[...] = jnp.full((2, 1, FOX_T), NEG_INF, F32)
            acc_s[...] = jnp.zeros((2, PACK, FOX_T), F32)

        def tile(diagonal):
            for hh in range(2):
                cols = slice(hh * PACK, (hh + 1) * PACK)
                st = lax.dot_general(k_ref[:, cols], q_ref[:, cols], nt_dims, preferred_element_type=F32)
                if diagonal:
                    key = lax.broadcasted_iota(jnp.int32, (FOX_T, FOX_T), 0)
                    qry = lax.broadcasted_iota(jnp.int32, (FOX_T, FOX_T), 1)
                    st = jnp.where(key <= qry, st, NEG_INF)
                m_old = m_s[hh]
                m_new = jnp.maximum(m_old, jnp.max(st, axis=0, keepdims=True))
                pt = jnp.exp(st - m_new)
                acc_s[hh] = jnp.exp(m_old - m_new) * acc_s[hh] + lax.dot_general(
                    v_ref[:, cols], pt.astype(_CD), tn_dims, preferred_element_type=F32)
                m_s[hh] = m_new

        @pl.when(j < i)
        def _():
            tile(False)

        @pl.when(j == i)
        def _():
            tile(True)

        @pl.when(j == nt - 1)
        def _():
            for hh in range(2):
                acc = acc_s[hh]
                den = acc[HEAD_DIM:HEAD_DIM + 1, :]
                cols = slice(hh * HEAD_DIM, (hh + 1) * HEAD_DIM)
                o_ref[:, cols] = (acc[:HEAD_DIM, :] / den).T
                l_ref[:, cols] = jnp.broadcast_to(m_s[hh] + jnp.log(den), (HEAD_DIM, FOX_T)).T

    qs = pl.BlockSpec((FOX_T, 2 * PACK), lambda hp, i, j: (i, hp))
    ks = pl.BlockSpec((FOX_T, 2 * PACK), lambda hp, i, j: (jnp.minimum(i, j), hp))
    os_ = pl.BlockSpec((FOX_T, PACK), lambda hp, i, j: (i, hp))
    shp = jax.ShapeDtypeStruct((S, FOX_WIDTH), F32)
    return _pcall(body, name=name, grid=(HEAD_PAIRS, nt, nt), in_specs=[qs, ks, ks], out_specs=[os_, os_],
                  out_shape=[shp, shp],
                  scratch_shapes=[pltpu.VMEM((2, 1, FOX_T), F32), pltpu.VMEM((2, PACK, FOX_T), F32)],
                  compiler_params=_params("parallel", "parallel", "arbitrary"))(qp, kp, vp)


def _fox_pack_bwd(qkv, c, o, lse, do, *, name, tm=512):
    S = qkv.shape[0]

    def body(q_ref, c_ref, o_ref, l_ref, do_ref, qo_ref, do_out_ref):
        hp = pl.program_id(1)
        cv = c_ref[...]
        for hh in range(2):
            src = slice(hh * HEAD_DIM, (hh + 1) * HEAD_DIM)
            lo = slice(hh * PACK, hh * PACK + HEAD_DIM)
            hi = slice(hh * PACK + HEAD_DIM, (hh + 1) * PACK)
            shift = _head_column(cv, 2 * hp + hh) - l_ref[:, hh * HEAD_DIM:hh * HEAD_DIM + 1]
            dov = do_ref[:, src]
            dsum = jnp.sum(dov * o_ref[:, src], axis=-1, keepdims=True)
            qo_ref[:, lo] = (q_ref[:, src].astype(F32) * ATTN_SCALE).astype(qo_ref.dtype)
            qo_ref[:, hi] = _extras(_pieces(shift), ONES3, tm).astype(qo_ref.dtype)
            do_out_ref[:, lo] = dov.astype(do_out_ref.dtype)
            do_out_ref[:, hi] = _extras(_pieces(-dsum), ZEROS3, tm).astype(do_out_ref.dtype)

    pair = pl.BlockSpec((tm, PACK), lambda i, hp: (i, hp))
    out = pl.BlockSpec((tm, 2 * PACK), lambda i, hp: (i, hp))
    shp = jax.ShapeDtypeStruct((S, N_FOX_HEADS * PACK), _CD)
    return _pcall(body, name=name, grid=(S // tm, HEAD_PAIRS),
                  in_specs=[pl.BlockSpec((tm, PACK), lambda i, hp: (i, Q_BLOCK0 + hp)),
                            pl.BlockSpec((tm, PACK), lambda i, hp: (i, 0)), pair, pair, pair],
                  out_specs=[out, out], out_shape=[shp, shp],
                  compiler_params=_params("parallel", "parallel"))(qkv, c, o, lse, do)


def _fox_bwd(qp, kp, vp, dop, *, name):
    S = qp.shape[0]
    nt = S // FOX_T
    nt_dims = (((1,), (1,)), ((), ()))
    tn_dims = (((0,), (0,)), ((), ()))

    def body(q_ref, k_ref, v_ref, do_ref, dq_ref, dk_ref, dv_ref, dc_ref, dr_ref, dq_s, dk_s, dv_s, dc_s, dr_s):
        j, i = pl.program_id(1), pl.program_id(2)

        @pl.when(jnp.logical_and(j == 0, i == 0))
        def _():
            dq_s[...] = jnp.zeros((S, 2 * PACK), F32)
            dr_s[...] = jnp.zeros((2, 1, S), F32)

        @pl.when(i == 0)
        def _():
            dk_s[...] = jnp.zeros((FOX_T, 2 * PACK), F32)
            dv_s[...] = jnp.zeros((FOX_T, 2 * PACK), F32)
            dc_s[...] = jnp.zeros((2, FOX_T, 1), F32)

        def tile(diagonal):
            rows = pl.ds(pl.multiple_of(i * FOX_T, FOX_T), FOX_T)
            for hh in range(2):
                cols = slice(hh * PACK, (hh + 1) * PACK)
                qv, kv, vv, dov = q_ref[:, cols], k_ref[:, cols], v_ref[:, cols], do_ref[:, cols]
                pt = jnp.exp(lax.dot_general(kv, qv, nt_dims, preferred_element_type=F32))
                if diagonal:
                    key = lax.broadcasted_iota(jnp.int32, (FOX_T, FOX_T), 0)
                    qry = lax.broadcasted_iota(jnp.int32, (FOX_T, FOX_T), 1)
                    pt = jnp.where(key <= qry, pt, 0.0)
                dst = pt * lax.dot_general(vv, dov, nt_dims, preferred_element_type=F32)
                dsb = dst.astype(_CD)
                dc_s[hh] += jnp.sum(dst, axis=1, keepdims=True)
                dr_s[hh, :, rows] += jnp.sum(dst, axis=0, keepdims=True)
                dv_s[:, cols] += jnp.dot(pt.astype(_CD), dov, preferred_element_type=F32)
                dk_s[:, cols] += jnp.dot(dsb, qv, preferred_element_type=F32)
                dq_s[rows, cols] += lax.dot_general(dsb, kv, tn_dims, preferred_element_type=F32)

        @pl.when(i > j)
        def _():
            tile(False)

        @pl.when(i == j)
        def _():
            tile(True)

        @pl.when(i == nt - 1)
        def _():
            dk_ref[...] = dk_s[...].astype(dk_ref.dtype)
            dv_ref[...] = dv_s[...].astype(dv_ref.dtype)
            for hh in range(2):
                dc_ref[:, hh * HEAD_DIM:(hh + 1) * HEAD_DIM] = jnp.broadcast_to(dc_s[hh], (FOX_T, HEAD_DIM))

        @pl.when(jnp.logical_and(j == nt - 1, i == nt - 1))
        def _():
            lane = lax.broadcasted_iota(jnp.int32, (S, 2 * PACK), 1) % PACK
            dq_ref[...] = (dq_s[...] * jnp.where(lane < HEAD_DIM, ATTN_SCALE, 1.0)).astype(dq_ref.dtype)
            dr_ref[...] = dr_s[...]

    qs = pl.BlockSpec((FOX_T, 2 * PACK), lambda hp, j, i: (jnp.maximum(i, j), hp))
    ks = pl.BlockSpec((FOX_T, 2 * PACK), lambda hp, j, i: (j, hp))
    whole = pl.BlockSpec((S, 2 * PACK), lambda hp, j, i: (0, hp))
    cs = pl.BlockSpec((FOX_T, PACK), lambda hp, j, i: (j, hp))
    rs = pl.BlockSpec((2, 1, S), lambda hp, j, i: (hp, 0, 0))
    shp = jax.ShapeDtypeStruct((S, N_FOX_HEADS * PACK), _CD)
    return _pcall(body, name=name, grid=(HEAD_PAIRS, nt, nt), in_specs=[qs, ks, ks, qs],
                  out_specs=[whole, ks, ks, cs, rs],
                  out_shape=[shp, shp, shp, jax.ShapeDtypeStruct((S, FOX_WIDTH), F32),
                             jax.ShapeDtypeStruct((N_FOX_HEADS, 1, S), F32)],
                  scratch_shapes=[pltpu.VMEM((S, 2 * PACK), F32), pltpu.VMEM((FOX_T, 2 * PACK), F32),
                                  pltpu.VMEM((FOX_T, 2 * PACK), F32), pltpu.VMEM((2, FOX_T, 1), F32),
                                  pltpu.VMEM((2, 1, S), F32)],
                  compiler_params=_params("parallel", "arbitrary", "arbitrary"))(qp, kp, vp, dop)


def _dedilate(t, d):
    if d == 1:
        return t
    S, C = t.shape
    return t.reshape(S // d, d, C).transpose(1, 0, 2).reshape(S, C)


def _redilate(t, d):
    if d == 1:
        return t
    S, C = t.shape
    return t.reshape(d, S // d, C).transpose(1, 0, 2).reshape(S, C)


def _layer_step(x, tgt, w, p, late_weights=None, grad_sink=None, after=None):
    S = x.shape[0]
    h = _rms_fwd(x, p["norm_mix_g"], name="rms_mix", after=after)
    qkv = _mm(h, w["qkv"], name="proj_qkv", out_dtype=_CD, tn=768)
    zf = _mm(h, w["f"], name="proj_f")
    gl = _mm(h, w["g"], name="proj_gate", tn=1024)

    dil_q, dil_k, dil_v = [], [], []
    dil_o, dil_l = [], []
    for g, (_, d) in enumerate(DIL_PAIRS):
        qg = _dedilate(qkv[:, g * DIL_OUT:(g + 1) * DIL_OUT], d)
        kg = _dedilate(qkv[:, DIL_WIDTH + g * DIL_OUT:DIL_WIDTH + (g + 1) * DIL_OUT], d)
        vg = _dedilate(qkv[:, 2 * DIL_WIDTH + g * DIL_OUT:2 * DIL_WIDTH + (g + 1) * DIL_OUT], d)
        og, lg = _dil_fwd(qg, kg, vg, g, name=f"dil_fwd{g}")
        dil_q.append(qg), dil_k.append(kg), dil_v.append(vg)
        dil_o.append(_redilate(og, d)), dil_l.append(_redilate(lg, d))
    o_a = _dil_mix_fwd(dil_o, dil_l, name="dil_mix")

    c = _fox_cumsum(zf, p["b_fgt"], name="fox_cumsum")
    fqp, fkp, fvp = _fox_pack_fwd(qkv, c, name="fox_pack")
    o_b, flse = _fox_fwd(fqp, fkp, fvp, name="fox_fwd")

    if late_weights is not None:
        w = {**w, **late_weights(o_b)}
    y_a = _mm(o_a, w["dil_out"], name="y_a", tn=1024)
    y_b = _mm(o_b, w["fox_out"], name="y_b", tn=1024)
    merged = _gate_fwd(gl, p["b_gate"], y_a, y_b, name="gate_fwd")
    x1 = _mm(merged, w["out"], name="mix_out", add=x)

    h2 = _rms_fwd(x1, p["norm_ffn_g"], name="rms_ffn")
    gu = _mm(h2, w["ffn_in"], name="ffn_in", tn=1408)
    act = _swiglu_fwd(gu, name="swiglu")
    x2 = _mm(act, w["ffn_down"], name="ffn_down", add=x1, tk=2816)

    loss, dx2, dg_final = _loss_head(x2, p["norm_final_g"], tgt, name="loss_head")

    dact = _mm(dx2, w["ffn_down"], name="d_act", tb=True, tn=1408)
    gw_ffn_down = _mm(act, dx2, name="gw_ffn_down", ta=True, out_dtype=_CD, tm=1408)
    dgu = _swiglu_bwd(dact, gu, name="swiglu_bwd")
    dh2 = _mm(dgu, w["ffn_in"], name="d_h2", tb=True, tk=2816)
    gw_ffn_in = _mm(h2, dgu, name="gw_ffn_in", ta=True, out_dtype=_CD)
    if grad_sink is not None:
        grad_sink("ffn", dict(ffn_in=gw_ffn_in, ffn_down=gw_ffn_down))
    dx1, dg_ffn = _rms_bwd(x1, p["norm_ffn_g"], dh2, dx2, name="rms_ffn_bwd")

    dmerged = _mm(dx1, w["out"], name="d_merged", tb=True)
    gw_out = _mm(merged, dx1, name="gw_out", ta=True, out_dtype=_CD)
    dy_a, dy_b, dgl, db_gate = _gate_bwd(dmerged, gl, p["b_gate"], y_a, y_b, name="gate_bwd")
    do_a = _mm(dy_a, w["dil_out"], name="d_o_a", tb=True)
    gw_dil_out = _mm(o_a, dy_a, name="gw_dil_out", ta=True, out_dtype=_CD, tn=1024)
    do_b = _mm(dy_b, w["fox_out"], name="d_o_b", tb=True)
    gw_fox_out = _mm(o_b, dy_b, name="gw_fox_out", ta=True, out_dtype=_CD, tn=1024)
    if grad_sink is not None:
        grad_sink("mix", dict(dil_out=gw_dil_out, fox_out=gw_fox_out, out=gw_out))

    bqp, bdop = _fox_pack_bwd(qkv, c, o_b, flse, do_b, name="fox_pack_bwd")
    dqp, dkp, dvp, dck, dcq = _fox_bwd(bqp, fkp, fvp, bdop, name="fox_bwd")
    dc = dcq[:, 0, :].T - dck.reshape(S, N_FOX_HEADS, HEAD_DIM)[:, :, 0]
    dc = jnp.pad(dc, ((0, 0), (0, F_PAD - N_FOX_HEADS)))
    dzf, db_fgt = _fox_cumsum_bwd(dc, zf, p["b_fgt"], name="fox_cumsum_bwd")

    def unpack(t):
        return t.reshape(S, N_FOX_HEADS, PACK)[:, :, :HEAD_DIM].reshape(S, FOX_WIDTH)

    douts = _dil_mix_bwd(do_a, dil_o, dil_l, name="dil_mix_bwd")
    dqs, dks, dvs = [], [], []
    for g, (_, d) in enumerate(DIL_PAIRS):
        dq, dk, dv = _dil_bwd(dil_q[g], dil_k[g], dil_v[g], _dedilate(dil_o[g], d), _dedilate(dil_l[g], d),
                              _dedilate(douts[g], d), _dedilate(douts[3 + g], d), g, name=f"dil_bwd{g}")
        dqs.append(_redilate(dq, d)), dks.append(_redilate(dk, d)), dvs.append(_redilate(dv, d))
    dqkv = jnp.concatenate(dqs + dks + dvs + [unpack(dqp), unpack(dkp), unpack(dvp)], axis=1)

    gw_qkv = _mm(h, dqkv, name="gw_qkv", ta=True, out_dtype=_CD, tn=768)
    gw_g = _mm(h, dgl, name="gw_gate", ta=True, out_dtype=_CD)
    gw_f = _mm(h, dzf, name="gw_f", ta=True, out_dtype=_CD)
    if grad_sink is not None:
        grad_sink("in", dict(qkv=gw_qkv, f=gw_f, g=gw_g))
    dh = _mm(dqkv, w["qkv"], name="d_h_qkv", tb=True, tk=1920)
    dh = _mm(dgl, w["g"], name="d_h_gate", tb=True, add=dh)
    dh = _mm(dzf, w["f"], name="d_h_f", tb=True, add=dh)
    dx, dg_mix = _rms_bwd(x, p["norm_mix_g"], dh, dx1, name="rms_mix_bwd")

    gw = dict(qkv=gw_qkv, f=gw_f, g=gw_g, dil_out=gw_dil_out, fox_out=gw_fox_out, out=gw_out, ffn_in=gw_ffn_in,
              ffn_down=gw_ffn_down)
    small = dict(norm_mix_g=dg_mix, b_fgt=db_fgt, b_gate=db_gate, norm_ffn_g=dg_ffn, norm_final_g=dg_final)
    return loss, dx, gw, small


def _position():
    return lax.axis_index("x"), lax.axis_index("y"), lax.axis_index("c")


def _other_chips(x, y):
    return [(1 - x, y), (x, 1 - y), (1 - x, 1 - y)]


ROW_TILE = 16


def _row_chunks(rows, want=4):
    n = want
    while n > 1 and rows % (n * ROW_TILE):
        n //= 2
    return n


def _gather_weights(shards):
    n = len(shards)
    nq = max(_row_chunks(s.shape[0] // 2) for s in shards)

    def body(*refs):
        ins, outs = refs[:n], refs[n:2 * n]
        send_sems, recv_sems = refs[2 * n:]
        x, y, c = _position()
        mine = 2 * x + y
        chips = _other_chips(x, y)
        started = []

        def pieces(w, core):
            half = ins[w].shape[0] // 2
            size = half // _row_chunks(half)
            return [pl.ds(core * half + q * size, size) for q in range(_row_chunks(half))]

        for w in range(n):
            for r, (cx, cy) in enumerate(chips):
                for q, rows in enumerate(pieces(w, c)):
                    cp = pltpu.make_async_remote_copy(
                        src_ref=ins[w].at[rows, :], dst_ref=outs[w].at[mine, rows, :], send_sem=send_sems.at[w, r, q],
                        recv_sem=recv_sems.at[w, r, q], device_id=(cx, cy, c), device_id_type=MESH)
                    cp.start()
                    started.append(cp)

        def landed(w, r, q, rows, peer):
            cx, cy = chips[r % 3]
            blk = outs[w].at[2 * cx + cy, rows, :]
            return pltpu.make_async_remote_copy(src_ref=blk, dst_ref=blk, send_sem=send_sems.at[w, r, q],
                                                recv_sem=recv_sems.at[w, r, q], device_id=peer, device_id_type=MESH)

        for w in range(n):
            for r, (cx, cy) in enumerate(chips):
                for q, rows in enumerate(pieces(w, c)):
                    landed(w, r, q, rows, (cx, cy, c)).wait_recv()
                    fwd = landed(w, 3 + r, q, rows, (x, y, 1 - c))
                    fwd.start()
                    started.append(fwd)
        for w in range(n):
            for r in range(3):
                for q, rows in enumerate(pieces(w, 1 - c)):
                    landed(w, 3 + r, q, rows, (x, y, 1 - c)).wait_recv()
        for cp in started:
            cp.wait_send()

    return _pcall(
        body, name="gather_weights", in_specs=[HBM_SPEC] * n, out_specs=[HBM_SPEC] * n,
        out_shape=[jax.ShapeDtypeStruct((4,) + s.shape, s.dtype) for s in shards],
        scratch_shapes=[pltpu.SemaphoreType.DMA((n, 6, nq)), pltpu.SemaphoreType.DMA((n, 6, nq))],
    )(*shards)


SEM_SPEC = pl.BlockSpec(memory_space=pltpu.SEMAPHORE)
ANY_SPEC = pl.BlockSpec(memory_space=pl.ANY)
DATAFLOW = pltpu.SideEffectType.DATAFLOW_SIDE_EFFECTING


def _in_hbm(a):
    return pltpu.with_memory_space_constraint(a, pltpu.HBM)


def _split_copy_start(srcs, land_shapes, copies, after, *, name):
    n, m = len(srcs), len(land_shapes)

    def body(*refs):
        src_refs, land_refs = refs[:n], refs[n:n + m]
        send_sems, recv_sems = refs[n + m + 1], refs[n + m + 2]
        token = refs[-1]
        x, y, c = _position()
        for k, (src, dst, peer) in enumerate(copies(x, y, c, src_refs, land_refs)):
            pltpu.make_async_remote_copy(src_ref=src, dst_ref=dst, send_sem=send_sems.at[k], recv_sem=recv_sems.at[k],
                                         device_id=peer, device_id_type=MESH).start()
        token[...] = jnp.zeros_like(token)

    lands = [lax.empty(s.shape, s.dtype) for s in land_shapes]
    count = len(copies(0, 0, 0, srcs, lands))
    out = _pcall(
        body, name=name,
        out_shape=(pltpu.SemaphoreType.DMA((count,)), pltpu.SemaphoreType.DMA((count,)),
                   *[pltpu.HBM(s.shape, s.dtype) for s in srcs], *[pltpu.HBM(s.shape, s.dtype) for s in land_shapes],
                   jax.ShapeDtypeStruct((8, 128), F32)),
        in_specs=[HBM_SPEC] * (n + m) + [ANY_SPEC],
        out_specs=(SEM_SPEC, SEM_SPEC, *[HBM_SPEC] * (n + m), pl.BlockSpec(memory_space=pltpu.VMEM)),
        input_output_aliases={k: 2 + k for k in range(n + m)},
        compiler_params=pltpu.CompilerParams(has_side_effects=DATAFLOW),
    )(*[_in_hbm(s) for s in srcs], *[_in_hbm(l) for l in lands], after)
    return out[0], out[1], list(out[2:2 + n]), list(out[2 + n:2 + n + m]), out[-1]


def _split_copy_wait(send_sems, recv_sems, srcs, lands, copies, after, *, name):
    n, m = len(srcs), len(lands)

    def body(*refs):
        src_refs, land_refs = refs[:n], refs[n:n + m]
        send, recv = refs[n + m], refs[n + m + 1]
        x, y, c = _position()
        for k, (src, dst, peer) in enumerate(copies(x, y, c, src_refs, land_refs)):
            cp = pltpu.make_async_remote_copy(src_ref=src, dst_ref=dst, send_sem=send.at[k], recv_sem=recv.at[k],
                                              device_id=peer, device_id_type=MESH)
            cp.wait_send()
            cp.wait_recv()

    out = _pcall(
        body, name=name,
        out_shape=tuple(pltpu.HBM(s.shape, s.dtype) for s in list(srcs) + list(lands)),
        in_specs=[HBM_SPEC] * (n + m) + [SEM_SPEC, SEM_SPEC, ANY_SPEC], out_specs=tuple([HBM_SPEC] * (n + m)),
        input_output_aliases={k: k for k in range(n + m)},
        compiler_params=pltpu.CompilerParams(has_side_effects=DATAFLOW),
    )(*srcs, *lands, send_sems, recv_sems, after)
    return list(out[:n]), list(out[n:])


def _gather_copies(x, y, c, shard_refs, land_refs):
    out = []
    for s, l in zip(shard_refs, land_refs):
        half = s.shape[0] // 2
        nq = _row_chunks(half)
        for cx, cy in _other_chips(x, y):
            for q in range(nq):
                rows = pl.ds(c * half + q * (half // nq), half // nq)
                out.append((s.at[rows, :], l.at[2 * x + y, rows, :], (cx, cy, c)))
    return out


def _scatter_copies(x, y, c, part_refs, land_refs):
    out = []
    for p, l in zip(part_refs, land_refs):
        nq = _row_chunks(p.shape[1])
        for r, (cx, cy) in enumerate(_other_chips(x, y)):
            for q in range(nq):
                rows = pl.ds(q * (p.shape[1] // nq), p.shape[1] // nq)
                out.append((p.at[2 * cx + cy, rows, :], l.at[r, rows, :], (cx, cy, c)))
    return out


def _forward_halves(lands, *, name):
    n = len(lands)

    def body(*refs):
        ins = refs[:n]
        send_sems, recv_sems = refs[2 * n:]
        x, y, c = _position()
        copies = []
        for w in range(n):
            half = ins[w].shape[1] // 2
            for r, (cx, cy) in enumerate(_other_chips(x, y)):
                blk = ins[w].at[2 * cx + cy, pl.ds(c * half, half), :]
                cp = pltpu.make_async_remote_copy(src_ref=blk, dst_ref=blk, send_sem=send_sems.at[w, r],
                                                  recv_sem=recv_sems.at[w, r], device_id=(x, y, 1 - c),
                                                  device_id_type=MESH)
                cp.start()
                copies.append(cp)
        for w in range(n):
            half = ins[w].shape[1] // 2
            for r, (cx, cy) in enumerate(_other_chips(x, y)):
                blk = ins[w].at[2 * cx + cy, pl.ds((1 - c) * half, half), :]
                pltpu.make_async_remote_copy(src_ref=blk, dst_ref=blk, send_sem=send_sems.at[w, r],
                                             recv_sem=recv_sems.at[w, r], device_id=(x, y, 1 - c),
                                             device_id_type=MESH).wait_recv()
        for cp in copies:
            cp.wait_send()

    return _pcall(
        body, name=name, in_specs=[HBM_SPEC] * n, out_specs=[HBM_SPEC] * n,
        out_shape=[jax.ShapeDtypeStruct(l.shape, l.dtype) for l in lands],
        input_output_aliases={k: k for k in range(n)},
        scratch_shapes=[pltpu.SemaphoreType.DMA((n, 3)), pltpu.SemaphoreType.DMA((n, 3))],
    )(*lands)


def _swap_halves(grads, name="swap_halves"):
    n = len(grads)

    def body(*refs):
        ins, outs = refs[:n], refs[n:2 * n]
        send_sems, recv_sems = refs[2 * n:]
        x, y, c = _position()
        copies = []
        for w in range(n):
            half = ins[w].shape[1] // 2
            cp = pltpu.make_async_remote_copy(
                src_ref=ins[w].at[:, pl.ds((1 - c) * half, half), :], dst_ref=outs[w], send_sem=send_sems.at[w],
                recv_sem=recv_sems.at[w], device_id=(x, y, 1 - c), device_id_type=MESH)
            cp.start()
            copies.append(cp)
        for cp in copies:
            cp.wait()

    return _pcall(
        body, name=name, in_specs=[HBM_SPEC] * n, out_specs=[HBM_SPEC] * n,
        out_shape=[jax.ShapeDtypeStruct((4, g.shape[1] // 2, g.shape[2]), g.dtype) for g in grads],
        scratch_shapes=[pltpu.SemaphoreType.DMA((n,)), pltpu.SemaphoreType.DMA((n,))],
    )(*grads)


def _share_halves(halves):
    n = len(halves)

    def body(*refs):
        ins, outs = refs[:n], refs[n:2 * n]
        send_sems, recv_sems = refs[2 * n:]
        x, y, c = _position()
        copies = []
        for w in range(n):
            cp = pltpu.make_async_remote_copy(src_ref=ins[w], dst_ref=outs[w], send_sem=send_sems.at[w],
                                              recv_sem=recv_sems.at[w], device_id=(x, y, 1 - c), device_id_type=MESH)
            cp.start()
            copies.append(cp)
        for cp in copies:
            cp.wait()

    return _pcall(
        body, name="share_halves", in_specs=[HBM_SPEC] * n, out_specs=[HBM_SPEC] * n,
        out_shape=[jax.ShapeDtypeStruct(h.shape, h.dtype) for h in halves],
        scratch_shapes=[pltpu.SemaphoreType.DMA((n,)), pltpu.SemaphoreType.DMA((n,))],
    )(*halves)


def _sum_small(part):
    rows, width = part.shape

    def body(x_ref, out_ref, all_ref, send_sems, recv_sems):
        x, y, c = _position()
        me, sibling = (x, y, c), (x, y, 1 - c)
        chips = _other_chips(x, y)

        def block(px, py, pc):
            return all_ref.at[pl.ds((4 * px + 2 * py + pc) * rows, rows), :]

        def copy(k, blk, to, src=None):
            return pltpu.make_async_remote_copy(
                src_ref=block(*blk) if src is None else src, dst_ref=block(*blk), send_sem=send_sems.at[k],
                recv_sem=recv_sems.at[k], device_id=to, device_id_type=MESH)

        all_ref[pl.ds((4 * x + 2 * y + c) * rows, rows), :] = x_ref[...]
        first = [copy(0, me, sibling, src=x_ref)]
        first += [copy(1 + j, me, (*chip, c), src=x_ref) for j, chip in enumerate(chips)]
        for cp in first:
            cp.start()
        passed = [copy(4 + j, (*chip, c), sibling) for j, chip in enumerate(chips)]
        for j, chip in enumerate(chips):
            copy(1 + j, (*chip, c), me).wait_recv()
            passed[j].start()
        copy(0, sibling, me).wait_recv()
        for j, chip in enumerate(chips):
            copy(4 + j, (*chip, 1 - c), me).wait_recv()
        for cp in first + passed:
            cp.wait_send()
        total = all_ref[0:rows, :]
        for d in range(1, 8):
            total = total + all_ref[d * rows:(d + 1) * rows, :]
        out_ref[...] = total

    vm = pl.BlockSpec(memory_space=pltpu.VMEM)
    return _pcall(
        body, name="sum_small", in_specs=[vm], out_specs=vm, out_shape=jax.ShapeDtypeStruct((rows, width), F32),
        scratch_shapes=[pltpu.VMEM((8 * rows, width), F32), pltpu.SemaphoreType.DMA((7,)), pltpu.SemaphoreType.DMA((7,))],
    )(part)


def _row_tile(R, C, itemsize=4, budget=1 << 20):
    for t in (512, 256, 128, 64, 32, 16, 8):
        if R % t == 0 and t * C * itemsize <= budget:
            return t
    return R


def _add_halves(g, recv, c, *, name):
    _, R, C = g.shape
    half = R // 2
    t = _row_tile(half, C)
    nb = half // t

    def body(c_ref, g_ref, r_ref, o_ref):
        o_ref[...] = (g_ref[...].astype(F32) + r_ref[...].astype(F32)).astype(o_ref.dtype)

    grid_spec = pltpu.PrefetchScalarGridSpec(
        num_scalar_prefetch=1, grid=(4, nb),
        in_specs=[pl.BlockSpec((1, t, C), lambda k, i, cr: (k, cr[0] * nb + i, 0)),
                  pl.BlockSpec((1, t, C), lambda k, i, cr: (k, i, 0))],
        out_specs=pl.BlockSpec((1, t, C), lambda k, i, cr: (k, i, 0)))
    return _pcall(body, name=name, grid_spec=grid_spec, out_shape=jax.ShapeDtypeStruct((4, half, C), g.dtype),
                  compiler_params=_params("parallel", "parallel"))(c, g, recv)


def _add_owners(mine, recv, *, name):
    half, C = mine.shape
    t = _row_tile(half, C)

    def body(m_ref, r_ref, o_ref):
        o_ref[...] = ((m_ref[...].astype(F32) + r_ref[0].astype(F32)) + r_ref[1].astype(F32)) + r_ref[2].astype(F32)

    return _pcall(body, name=name, grid=(half // t,),
                  in_specs=[pl.BlockSpec((t, C), lambda i: (i, 0)), pl.BlockSpec((3, t, C), lambda i: (0, i, 0))],
                  out_specs=pl.BlockSpec((t, C), lambda i: (i, 0)), out_shape=jax.ShapeDtypeStruct((half, C), F32),
                  compiler_params=_params("parallel"))(mine, recv)


def _adamw(w, g, m, v, *, name):
    R, C = w.shape
    t = _row_tile(R, C)
    c1 = 1.0 - ADAM_B1 ** ADAM_STEP
    c2 = 1.0 - ADAM_B2 ** ADAM_STEP

    def body(w_ref, g_ref, m_ref, v_ref, d_ref, nm_ref, nv_ref):
        gv = g_ref[...]
        mn = ADAM_B1 * m_ref[...] + (1.0 - ADAM_B1) * gv
        vn = ADAM_B2 * v_ref[...] + (1.0 - ADAM_B2) * (gv * gv)
        d_ref[...] = -ADAM_LR * ((mn / c1) / (jnp.sqrt(vn / c2) + ADAM_EPS) + ADAM_WD * w_ref[...])
        nm_ref[...] = mn
        nv_ref[...] = vn

    blk = pl.BlockSpec((t, C), lambda i: (i, 0))
    shp = jax.ShapeDtypeStruct((R, C), F32)
    return _pcall(body, name=name, grid=(R // t,), in_specs=[blk] * 4, out_specs=[blk] * 3, out_shape=[shp] * 3,
                  compiler_params=_params("parallel"))(w, g, m, v)


BIG = ("w_in", "w_dil_out", "w_fox_out", "w_out", "w_ffn_in", "w_ffn_down")
SMALL = ("norm_mix_g", "b_fgt", "b_gate", "norm_ffn_g", "norm_final_g")
ORDER = ("norm_mix_g", "w_in", "b_fgt", "b_gate", "w_dil_out", "w_fox_out", "w_out", "norm_ffn_g", "w_ffn_in",
         "w_ffn_down", "norm_final_g")
SMALL_ROWS = {"norm_mix_g": (0, 1), "b_gate": (1, 3), "norm_ffn_g": (3, 4), "norm_final_g": (4, 5), "b_fgt": (5, 6)}


def _columns_to_blocks(full, ncol):
    K = full.shape[0]
    return full.reshape(K, 4, ncol).transpose(1, 0, 2)


def _blocks_to_columns(blocks):
    n, K, ncol = blocks.shape
    return blocks.transpose(1, 0, 2).reshape(K, n * ncol)


def kernel(x, norm_mix_g, w_in, b_fgt, b_gate, w_dil_out, w_fox_out, w_out, norm_ffn_g, w_ffn_in, w_ffn_down, norm_final_g, loss_target, m_norm_mix_g, m_w_in, m_b_fgt, m_b_gate, m_w_dil_out, m_w_fox_out, m_w_out, m_norm_ffn_g, m_w_ffn_in, m_w_ffn_down, m_norm_final_g, v_norm_mix_g, v_w_in, v_b_fgt, v_b_gate, v_w_dil_out, v_w_fox_out, v_w_out, v_norm_ffn_g, v_w_ffn_in, v_w_ffn_down, v_norm_final_g):
    weights = dict(norm_mix_g=norm_mix_g, w_in=w_in, b_fgt=b_fgt, b_gate=b_gate, w_dil_out=w_dil_out,
                   w_fox_out=w_fox_out, w_out=w_out, norm_ffn_g=norm_ffn_g, w_ffn_in=w_ffn_in, w_ffn_down=w_ffn_down,
                   norm_final_g=norm_final_g)
    m_in = dict(norm_mix_g=m_norm_mix_g, w_in=m_w_in, b_fgt=m_b_fgt, b_gate=m_b_gate, w_dil_out=m_w_dil_out,
                w_fox_out=m_w_fox_out, w_out=m_w_out, norm_ffn_g=m_norm_ffn_g, w_ffn_in=m_w_ffn_in,
                w_ffn_down=m_w_ffn_down, norm_final_g=m_norm_final_g)
    v_in = dict(norm_mix_g=v_norm_mix_g, w_in=v_w_in, b_fgt=v_b_fgt, b_gate=v_b_gate, w_dil_out=v_w_dil_out,
                w_fox_out=v_w_fox_out, w_out=v_w_out, norm_ffn_g=v_norm_ffn_g, w_ffn_in=v_w_ffn_in,
                w_ffn_down=v_w_ffn_down, norm_final_g=v_norm_final_g)
    c = lax.axis_index("c")
    chip = 2 * lax.axis_index("x") + lax.axis_index("y")

    shards = {n: weights[n][0].astype(_CD) for n in BIG}
    (g_in,) = _gather_weights([shards["w_in"]])
    late = BIG[1:]
    send_g, recv_g, late_src, late_land, token = _split_copy_start(
        [shards[n] for n in late], [jax.ShapeDtypeStruct((4,) + shards[n].shape, _CD) for n in late],
        _gather_copies, g_in, name="gather_late_start")
    full_in = _blocks_to_columns(lax.dynamic_update_index_in_dim(g_in, shards["w_in"], chip, 0))
    o3 = QKV_COLS
    o4 = o3 + N_FOX_HEADS
    w = dict(qkv=full_in[:, :o3], f=jnp.pad(full_in[:, o3:o4], ((0, 0), (0, F_PAD - N_FOX_HEADS))), g=full_in[:, o4:])
    p = dict(norm_mix_g=norm_mix_g, b_fgt=jnp.pad(b_fgt, ((0, 0), (0, F_PAD - N_FOX_HEADS))), b_gate=b_gate,
             norm_ffn_g=norm_ffn_g, norm_final_g=norm_final_g.reshape(1, D_MODEL))

    def late_weights(after):
        own, lands = _split_copy_wait(send_g, recv_g, late_src, late_land, _gather_copies, after,
                                      name="gather_late_wait")
        lands = _forward_halves(lands, name="gather_late_forward")
        g_dil, g_fox, g_out, g_ffn_in, g_ffn_down = [
            lax.dynamic_update_index_in_dim(l, s, chip, 0) for l, s in zip(lands, own)]
        return dict(dil_out=_blocks_to_columns(g_dil), fox_out=_blocks_to_columns(g_fox),
                    out=g_out.reshape(D_MODEL, D_MODEL), ffn_in=_blocks_to_columns(g_ffn_in),
                    ffn_down=g_ffn_down.reshape(D_FF, D_MODEL))

    c_arr = jnp.reshape(c, (1,)).astype(jnp.int32)

    def to_blocks(n, full):
        shape = weights[n].shape
        if n in ("w_out", "w_ffn_down"):
            return full.reshape(4, shape[1], shape[2])
        return _columns_to_blocks(full, shape[2])

    def pair_sums(group, named):
        names = list(named)
        blocks = [to_blocks(n, named[n]) for n in names]
        from_sibling = _swap_halves(blocks, name=f"swap_halves_{group}")
        return [_add_halves(b, r, c_arr, name=f"add_halves_{n}") for b, r, n in zip(blocks, from_sibling, names)]

    in_flight = {}

    def grad_sink(group, gw):
        if group == "in":
            named = {"w_in": jnp.concatenate([gw["qkv"], gw["f"][:, :N_FOX_HEADS], gw["g"]], axis=1)}
        else:
            named = {"w_" + k: v for k, v in gw.items()}
        sums = pair_sums(group, named)
        started = _split_copy_start(sums, [jax.ShapeDtypeStruct((3,) + s.shape[1:], s.dtype) for s in sums],
                                    _scatter_copies, next(iter(gw.values())), name=f"scatter_{group}_start")
        in_flight[group] = (list(named), started)

    loss_part, grad_x, gw, small = _layer_step(x[0], loss_target[0], w, p, late_weights, grad_sink, token)

    def owner_sums(names, sums, from_chips):
        return {n: _add_owners(lax.dynamic_index_in_dim(s, chip, 0, keepdims=False), r, name=f"add_owners_{n}")
                for n, s, r in zip(names, sums, from_chips)}

    halves = {}
    for group, (names, (send_s, recv_s, srcs, lands, _)) in in_flight.items():
        sums, from_chips = _split_copy_wait(send_s, recv_s, srcs, lands, _scatter_copies, grad_x,
                                            name=f"scatter_{group}_wait")
        halves.update(owner_sums(names, sums, from_chips))
    halves = [halves[n] for n in BIG]
    grads = {}
    for n, own, other in zip(BIG, halves, _share_halves(halves)):
        pair = jnp.stack([own, other])
        grads[n] = jnp.where(c == 0, pair, pair[::-1]).reshape(2 * own.shape[0], own.shape[1])

    packed = jnp.concatenate([
        small["norm_mix_g"], small["b_gate"].reshape(2, D_MODEL), small["norm_ffn_g"], small["norm_final_g"],
        jnp.pad(small["b_fgt"], ((0, 0), (0, D_MODEL - F_PAD))), jnp.zeros((2, D_MODEL), F32)], axis=0)
    summed = _sum_small(packed)
    for n in SMALL:
        lo, hi = SMALL_ROWS[n]
        grads[n] = summed[lo:hi].reshape(1, -1)[:, :weights[n].size]

    loss = lax.psum(loss_part[0, 0], ("x", "y", "c"))

    out_g, out_d, out_m, out_v = {}, {}, {}, {}
    for n in ORDER:
        shape = weights[n].shape
        two_d = shape[1:] if len(shape) == 3 else (1, weights[n].size)
        g2 = grads[n].reshape(two_d)
        d2, m2, v2 = _adamw(weights[n].reshape(two_d), g2, m_in[n].reshape(two_d), v_in[n].reshape(two_d),
                            name=f"adamw_{n}")
        out_g[n], out_d[n], out_m[n], out_v[n] = (g2.reshape(shape), d2.reshape(shape), m2.reshape(shape),
                                                  v2.reshape(shape))
    return (loss, grad_x[None], *[out_g[n] for n in ORDER], *[out_d[n] for n in ORDER],
            *[out_m[n] for n in ORDER], *[out_v[n] for n in ORDER])
```

```python
import numpy as np
import jax
import jax.numpy as jnp
from jax import lax
from jax.experimental import pallas as pl
from jax.experimental.pallas import tpu as pltpu

F32 = jnp.float32
_CD = jnp.bfloat16

D_MODEL = 1024
HEAD_DIM = 64
DIL_PAIRS = ((128, 1), (512, 4), (2048, 16))
N_DIL_GROUPS = 3
DIL_HEADS = 4
DIL_W = 128
DIL_OUT = DIL_HEADS * HEAD_DIM
DIL_WIDTH = N_DIL_GROUPS * DIL_OUT
N_FOX_HEADS = 8
FOX_WIDTH = N_FOX_HEADS * HEAD_DIM
D_FF = 2816
QKV_COLS = 3 * DIL_WIDTH + 3 * FOX_WIDTH
F_PAD = 128
RMS_EPS = 1e-6
NEG_INF = -1e30
ATTN_SCALE = HEAD_DIM ** -0.5
ADAM_LR, ADAM_B1, ADAM_B2, ADAM_EPS, ADAM_WD, ADAM_STEP = 0.001, 0.9, 0.999, 1e-08, 0.01, 10

VMEM_LIMIT = 48 * 1024 * 1024
MESH = pl.DeviceIdType.MESH
HBM_SPEC = pl.BlockSpec(memory_space=pltpu.HBM)


def _pcall(body, after=None, **kw):
    if after is None:
        return pl.pallas_call(body, **kw)
    n_in = len(kw["in_specs"])
    kw["in_specs"] = list(kw["in_specs"]) + [pl.BlockSpec(memory_space=pl.ANY)]

    def tied(*refs):
        return body(*refs[:n_in], *refs[n_in + 1:])

    call = pl.pallas_call(tied, **kw)
    return lambda *args: call(*args, after)


def _params(*sem):
    return pltpu.CompilerParams(dimension_semantics=sem, vmem_limit_bytes=VMEM_LIMIT)


def _pick(dim, pref):
    t = (min(pref, dim) // 128) * 128
    while t >= 128:
        if dim % t == 0:
            return t
        t -= 128
    return dim


def _mm(a, b, *, name, ta=False, tb=False, out_dtype=F32, add=None, tm=1024, tn=512, tk=2048, after=None):
    if ta:
        K, M = a.shape
    else:
        M, K = a.shape
    if tb:
        N, K2 = b.shape
    else:
        K2, N = b.shape
    assert K == K2, (a.shape, b.shape)
    tm, tn, tk = _pick(M, tm), _pick(N, tn), _pick(K, tk)
    nk = K // tk
    dn = (((0 if ta else 1,), (1 if tb else 0,)), ((), ()))
    has_add = add is not None

    def body(*refs):
        a_ref, b_ref = refs[0], refs[1]
        add_ref = refs[2] if has_add else None
        o_ref = refs[3] if has_add else refs[2]
        p = lax.dot_general(a_ref[...].astype(_CD), b_ref[...].astype(_CD), dn, preferred_element_type=F32)

        def finish(r):
            if has_add:
                r = r + add_ref[...]
            o_ref[...] = r.astype(out_dtype)

        if nk == 1:
            finish(p)
        else:
            acc_ref = refs[-1]
            k = pl.program_id(2)

            @pl.when(k == 0)
            def _():
                acc_ref[...] = p

            @pl.when(k > 0)
            def _():
                acc_ref[...] += p

            @pl.when(k == nk - 1)
            def _():
                finish(acc_ref[...])

    a_spec = pl.BlockSpec((tk, tm), lambda i, j, k: (k, i)) if ta else pl.BlockSpec((tm, tk), lambda i, j, k: (i, k))
    b_spec = pl.BlockSpec((tn, tk), lambda i, j, k: (j, k)) if tb else pl.BlockSpec((tk, tn), lambda i, j, k: (k, j))
    o_spec = pl.BlockSpec((tm, tn), lambda i, j, k: (i, j))
    in_specs = [a_spec, b_spec] + ([o_spec] if has_add else [])
    args = (a, b) + ((add,) if has_add else ())
    return _pcall(
        body, after, name=name, grid=(M // tm, N // tn, nk), in_specs=in_specs, out_specs=o_spec,
        out_shape=jax.ShapeDtypeStruct((M, N), out_dtype),
        scratch_shapes=[pltpu.VMEM((tm, tn), F32)] if nk > 1 else [],
        compiler_params=_params("parallel", "parallel", "arbitrary"),
    )(*args)


def _rms_fwd(x, g, *, name, tm=512, after=None):
    S, D = x.shape

    def body(x_ref, g_ref, h_ref):
        xv = x_ref[...]
        r = lax.rsqrt(jnp.mean(xv * xv, axis=-1, keepdims=True) + RMS_EPS)
        h_ref[...] = ((xv * r) * g_ref[...]).astype(h_ref.dtype)

    row = pl.BlockSpec((tm, D), lambda i: (i, 0))
    return _pcall(body, after, name=name, grid=(S // tm,), in_specs=[row, pl.BlockSpec((1, D), lambda i: (0, 0))],
                  out_specs=row, out_shape=jax.ShapeDtypeStruct((S, D), _CD), compiler_params=_params("parallel"))(x, g)


def _rms_bwd(x, g, dh, dres, *, name, tm=512, after=None):
    S, D = x.shape

    def body(x_ref, g_ref, dh_ref, dres_ref, dx_ref, dg_ref):
        xv = x_ref[...]
        r = lax.rsqrt(jnp.mean(xv * xv, axis=-1, keepdims=True) + RMS_EPS)
        xh = xv * r
        dhv = dh_ref[...]
        dxh = dhv * g_ref[...]
        dx_ref[...] = dres_ref[...] + r * (dxh - xh * jnp.mean(dxh * xh, axis=-1, keepdims=True))
        part = jnp.sum(dhv * xh, axis=0, keepdims=True)

        @pl.when(pl.program_id(0) == 0)
        def _():
            dg_ref[...] = part

        @pl.when(pl.program_id(0) > 0)
        def _():
            dg_ref[...] += part

    row = pl.BlockSpec((tm, D), lambda i: (i, 0))
    vec = pl.BlockSpec((1, D), lambda i: (0, 0))
    return _pcall(body, after, name=name, grid=(S // tm,), in_specs=[row, vec, row, row], out_specs=[row, vec],
                  out_shape=[jax.ShapeDtypeStruct((S, D), F32), jax.ShapeDtypeStruct((1, D), F32)],
                  compiler_params=_params("arbitrary"))(x, g, dh, dres)


def _loss_head(x, g, tgt, *, name, tm=512):
    S, D = x.shape

    def body(x_ref, g_ref, t_ref, loss_ref, dx_ref, dg_ref):
        xv = x_ref[...]
        gv = g_ref[...]
        r = lax.rsqrt(jnp.mean(xv * xv, axis=-1, keepdims=True) + RMS_EPS)
        xh = xv * r
        err = xh * gv - t_ref[...]
        lpart = 0.5 * jnp.sum(jnp.mean(err * err, axis=-1, keepdims=True), axis=0, keepdims=True)
        dy = err * (1.0 / D)
        dxh = dy * gv
        dx_ref[...] = r * (dxh - xh * jnp.mean(dxh * xh, axis=-1, keepdims=True))
        gpart = jnp.sum(dy * xh, axis=0, keepdims=True)

        @pl.when(pl.program_id(0) == 0)
        def _():
            loss_ref[...] = lpart
            dg_ref[...] = gpart

        @pl.when(pl.program_id(0) > 0)
        def _():
            loss_ref[...] += lpart
            dg_ref[...] += gpart

    row = pl.BlockSpec((tm, D), lambda i: (i, 0))
    vec = pl.BlockSpec((1, D), lambda i: (0, 0))
    one = pl.BlockSpec((1, 1), lambda i: (0, 0))
    return _pcall(body, name=name, grid=(S // tm,), in_specs=[row, vec, row], out_specs=[one, row, vec],
                  out_shape=[jax.ShapeDtypeStruct((1, 1), F32), jax.ShapeDtypeStruct((S, D), F32),
                             jax.ShapeDtypeStruct((1, D), F32)],
                  compiler_params=_params("arbitrary"))(x, g, tgt)


def _sigmoid(z):
    return 1.0 / (1.0 + jnp.exp(-z))


def _gate_fwd(gl, bg, ya, yb, *, name, tm=512):
    S, D = ya.shape

    def body(za_ref, zb_ref, ba_ref, bb_ref, ya_ref, yb_ref, o_ref):
        ga = _sigmoid(za_ref[...] + ba_ref[...])
        gb = _sigmoid(zb_ref[...] + bb_ref[...])
        o_ref[...] = (ga * ya_ref[...] + gb * yb_ref[...]).astype(o_ref.dtype)

    lo = pl.BlockSpec((tm, D), lambda i: (i, 0))
    hi = pl.BlockSpec((tm, D), lambda i: (i, 1))
    vlo = pl.BlockSpec((1, D), lambda i: (0, 0))
    vhi = pl.BlockSpec((1, D), lambda i: (0, 1))
    return _pcall(body, name=name, grid=(S // tm,), in_specs=[lo, hi, vlo, vhi, lo, lo], out_specs=lo,
                  out_shape=jax.ShapeDtypeStruct((S, D), _CD), compiler_params=_params("parallel"))(gl, gl, bg, bg, ya, yb)


def _gate_bwd(dm, gl, bg, ya, yb, *, name, tm=256):
    S, D = ya.shape

    def body(dm_ref, za_ref, zb_ref, ba_ref, bb_ref, ya_ref, yb_ref, dya_ref, dyb_ref, dgl_ref, dbg_ref):
        dmv = dm_ref[...]
        ga = _sigmoid(za_ref[...] + ba_ref[...])
        gb = _sigmoid(zb_ref[...] + bb_ref[...])
        dya_ref[...] = (dmv * ga).astype(dya_ref.dtype)
        dyb_ref[...] = (dmv * gb).astype(dyb_ref.dtype)
        dza = dmv * ya_ref[...] * ga * (1.0 - ga)
        dzb = dmv * yb_ref[...] * gb * (1.0 - gb)
        dgl_ref[:, :D] = dza.astype(dgl_ref.dtype)
        dgl_ref[:, D:] = dzb.astype(dgl_ref.dtype)
        pa = jnp.sum(dza, axis=0, keepdims=True)
        pb = jnp.sum(dzb, axis=0, keepdims=True)

        @pl.when(pl.program_id(0) == 0)
        def _():
            dbg_ref[:, :D] = pa
            dbg_ref[:, D:] = pb

        @pl.when(pl.program_id(0) > 0)
        def _():
            dbg_ref[:, :D] += pa
            dbg_ref[:, D:] += pb

    lo = pl.BlockSpec((tm, D), lambda i: (i, 0))
    hi = pl.BlockSpec((tm, D), lambda i: (i, 1))
    vlo = pl.BlockSpec((1, D), lambda i: (0, 0))
    vhi = pl.BlockSpec((1, D), lambda i: (0, 1))
    wide = pl.BlockSpec((tm, 2 * D), lambda i: (i, 0))
    vwide = pl.BlockSpec((1, 2 * D), lambda i: (0, 0))
    return _pcall(body, name=name, grid=(S // tm,), in_specs=[lo, lo, hi, vlo, vhi, lo, lo],
                  out_specs=[lo, lo, wide, vwide],
                  out_shape=[jax.ShapeDtypeStruct((S, D), _CD), jax.ShapeDtypeStruct((S, D), _CD),
                             jax.ShapeDtypeStruct((S, 2 * D), _CD), jax.ShapeDtypeStruct((1, 2 * D), F32)],
                  compiler_params=_params("arbitrary"))(dm, gl, gl, bg, bg, ya, yb)


def _swiglu_fwd(gu, *, name, tm=256):
    S, F2 = gu.shape
    F = F2 // 2

    def body(g_ref, u_ref, o_ref):
        gv = g_ref[...]
        o_ref[...] = (gv * _sigmoid(gv) * u_ref[...]).astype(o_ref.dtype)

    lo = pl.BlockSpec((tm, F), lambda i: (i, 0))
    hi = pl.BlockSpec((tm, F), lambda i: (i, 1))
    return _pcall(body, name=name, grid=(S // tm,), in_specs=[lo, hi], out_specs=lo,
                  out_shape=jax.ShapeDtypeStruct((S, F), _CD), compiler_params=_params("parallel"))(gu, gu)


def _swiglu_bwd(dact, gu, *, name, tm=256):
    S, F2 = gu.shape
    F = F2 // 2

    def body(d_ref, g_ref, u_ref, o_ref):
        dv = d_ref[...]
        gv = g_ref[...]
        sg = _sigmoid(gv)
        o_ref[:, :F] = (dv * u_ref[...] * (sg * (1.0 + gv * (1.0 - sg)))).astype(o_ref.dtype)
        o_ref[:, F:] = (dv * (gv * sg)).astype(o_ref.dtype)

    lo = pl.BlockSpec((tm, F), lambda i: (i, 0))
    hi = pl.BlockSpec((tm, F), lambda i: (i, 1))
    return _pcall(body, name=name, grid=(S // tm,), in_specs=[lo, lo, hi],
                  out_specs=pl.BlockSpec((tm, F2), lambda i: (i, 0)),
                  out_shape=jax.ShapeDtypeStruct((S, F2), _CD), compiler_params=_params("parallel"))(dact, gu, gu)


def _split3(x):
    hi = x.astype(jnp.bfloat16)
    r1 = x - hi.astype(F32)
    mid = r1.astype(jnp.bfloat16)
    lo = (r1 - mid.astype(F32)).astype(jnp.bfloat16)
    return hi, mid, lo


def _ones_dot_left(ones, x):
    return sum(jnp.dot(ones, p, preferred_element_type=F32) for p in _split3(x))


def _ones_dot_right(x, ones):
    return sum(jnp.dot(p, ones, preferred_element_type=F32) for p in _split3(x))


def _head_sum(x):
    n = x.shape[1]
    r = lax.broadcasted_iota(jnp.int32, (n, n), 0) // HEAD_DIM
    c = lax.broadcasted_iota(jnp.int32, (n, n), 1) // HEAD_DIM
    return _ones_dot_right(x, (r == c).astype(jnp.bfloat16))


def _log_sigmoid(z):
    e = jnp.exp(-jnp.abs(z))
    t = 1.0 + e
    log1p_e = jnp.where(t == 1.0, e, jnp.log(t) * (e / jnp.where(t == 1.0, 1.0, t - 1.0)))
    return jnp.minimum(z, 0.0) - log1p_e


def _fox_cumsum(zf, bf, *, name):
    S, W = zf.shape
    nb = S // 128

    def body(z_ref, b_ref, c_ref):
        tri = (lax.broadcasted_iota(jnp.int32, (128, 128), 0) >= lax.broadcasted_iota(jnp.int32, (128, 128), 1))
        tri = tri.astype(jnp.bfloat16)

        def step(i, carry):
            rows = pl.ds(pl.multiple_of(i * 128, 128), 128)
            lf = _log_sigmoid(z_ref[rows, :] + b_ref[...])
            cb = _ones_dot_left(tri, lf) + carry
            c_ref[rows, :] = cb
            return cb[127:128, :]

        lax.fori_loop(0, nb, step, jnp.zeros((1, W), F32))

    return _pcall(body, name=name, out_shape=jax.ShapeDtypeStruct((S, W), F32),
                  compiler_params=pltpu.CompilerParams(vmem_limit_bytes=VMEM_LIMIT))(zf, bf)


def _fox_cumsum_bwd(dc, zf, bf, *, name):
    S, W = zf.shape
    nb = S // 128

    def body(dc_ref, z_ref, b_ref, dz_ref, db_ref):
        tri = (lax.broadcasted_iota(jnp.int32, (128, 128), 0) <= lax.broadcasted_iota(jnp.int32, (128, 128), 1))
        tri = tri.astype(jnp.bfloat16)

        def step(k, carry):
            tail, acc = carry
            i = nb - 1 - k
            rows = pl.ds(pl.multiple_of(i * 128, 128), 128)
            dlf = _ones_dot_left(tri, dc_ref[rows, :]) + tail
            dz = dlf * _sigmoid(-(z_ref[rows, :] + b_ref[...]))
            dz_ref[rows, :] = dz
            return dlf[0:1, :], acc + jnp.sum(dz, axis=0, keepdims=True)

        _, acc = lax.fori_loop(0, nb, step, (jnp.zeros((1, W), F32), jnp.zeros((1, W), F32)))
        db_ref[...] = acc

    return _pcall(body, name=name,
                  out_shape=[jax.ShapeDtypeStruct((S, W), F32), jax.ShapeDtypeStruct((1, W), F32)],
                  compiler_params=pltpu.CompilerParams(vmem_limit_bytes=VMEM_LIMIT))(dc, zf, bf)


def _dil_slopes(group):
    h = np.arange(1, N_DIL_GROUPS * DIL_HEADS + 1, dtype=np.float32)
    s = (np.float32(2.0) ** (np.float32(-8.0) * h / np.float32(N_DIL_GROUPS * DIL_HEADS))).astype(np.float32)
    return [float(v) for v in s.reshape(N_DIL_GROUPS, DIL_HEADS)[group]]


def _dil_tiles(i, n, blocks_per_seq):
    qi = lax.broadcasted_iota(jnp.int32, (DIL_W, DIL_W), 0)
    kj = lax.broadcasted_iota(jnp.int32, (DIL_W, DIL_W), 1)
    first = ((4 * n + i) % blocks_per_seq) == 0
    valid_prev = jnp.logical_and(kj >= qi, jnp.logical_not(first))
    valid_cur = kj <= qi
    rel_prev = (qi - kj + DIL_W).astype(F32)
    rel_cur = (qi - kj).astype(F32)
    return valid_prev, valid_cur, rel_prev, rel_cur


CHUNK = 4 * DIL_W


def _dil_fwd(q, k, v, group, *, name):
    S = q.shape[0]
    dilation = DIL_PAIRS[group][1]
    bps = (S // dilation) // DIL_W
    slopes = _dil_slopes(group)
    nt = (((1,), (1,)), ((), ()))

    def body(q_ref, k_ref, v_ref, kp_ref, vp_ref, o_ref, l_ref):
        n = pl.program_id(0)
        for i in range(4):
            valid_prev, valid_cur, rel_prev, rel_cur = _dil_tiles(i, n, bps)
            rows = slice(i * DIL_W, (i + 1) * DIL_W)
            prow = slice((i - 1) * DIL_W, i * DIL_W)
            for h in range(DIL_HEADS):
                cols = slice(h * HEAD_DIM, (h + 1) * HEAD_DIM)
                qh = q_ref[rows, cols]
                kc, vc = k_ref[rows, cols], v_ref[rows, cols]
                kp = kp_ref[:, cols] if i == 0 else k_ref[prow, cols]
                vp = vp_ref[:, cols] if i == 0 else v_ref[prow, cols]
                sl = slopes[h] * dilation
                sp = lax.dot_general(qh, kp, nt, preferred_element_type=F32) * ATTN_SCALE - sl * rel_prev
                sc = lax.dot_general(qh, kc, nt, preferred_element_type=F32) * ATTN_SCALE - sl * rel_cur
                sp = jnp.where(valid_prev, sp, NEG_INF)
                sc = jnp.where(valid_cur, sc, NEG_INF)
                m = jnp.maximum(jnp.max(sp, axis=-1, keepdims=True), jnp.max(sc, axis=-1, keepdims=True))
                pp, pc = jnp.exp(sp - m), jnp.exp(sc - m)
                den = jnp.sum(pp, axis=-1, keepdims=True) + jnp.sum(pc, axis=-1, keepdims=True)
                acc = (jnp.dot(pp.astype(_CD), vp, preferred_element_type=F32)
                       + jnp.dot(pc.astype(_CD), vc, preferred_element_type=F32))
                o_ref[rows, cols] = acc / den
                l_ref[rows, cols] = jnp.broadcast_to(m + jnp.log(den), (DIL_W, HEAD_DIM))

    cur = pl.BlockSpec((CHUNK, DIL_OUT), lambda n: (n, 0))
    prev = pl.BlockSpec((DIL_W, DIL_OUT), lambda n: (jnp.maximum(4 * n - 1, 0), 0))
    return _pcall(body, name=name, grid=(S // CHUNK,), in_specs=[cur, cur, cur, prev, prev], out_specs=[cur, cur],
                  out_shape=[jax.ShapeDtypeStruct((S, DIL_OUT), F32), jax.ShapeDtypeStruct((S, DIL_OUT), F32)],
                  compiler_params=_params("parallel"))(q, k, v, k, v)


def _dil_bwd(q, k, v, o, lse, do, dlse, group, *, name):
    S = q.shape[0]
    dilation = DIL_PAIRS[group][1]
    bps = (S // dilation) // DIL_W
    slopes = _dil_slopes(group)
    nchunk = S // CHUNK
    nt = (((1,), (1,)), ((), ()))
    tn = (((0,), (0,)), ((), ()))

    def body(q_ref, k_ref, v_ref, kp_ref, vp_ref, o_ref, l_ref, do_ref, dl_ref, dq_ref, dk_ref, dv_ref,
             dk_s, dv_s):
        step = pl.program_id(0)
        n = nchunk - 1 - step

        @pl.when(step == 0)
        def _():
            dk_s[CHUNK:, :] = jnp.zeros((DIL_W, DIL_OUT), F32)
            dv_s[CHUNK:, :] = jnp.zeros((DIL_W, DIL_OUT), F32)

        dk_s[:CHUNK, :] = jnp.zeros((CHUNK, DIL_OUT), F32)
        dv_s[:CHUNK, :] = jnp.zeros((CHUNK, DIL_OUT), F32)
        for i in range(4):
            valid_prev, valid_cur, rel_prev, rel_cur = _dil_tiles(i, n, bps)
            rows = slice(i * DIL_W, (i + 1) * DIL_W)
            prow = slice((i - 1) * DIL_W, i * DIL_W)
            s_prev = slice(i * DIL_W, (i + 1) * DIL_W)
            s_cur = slice((i + 1) * DIL_W, (i + 2) * DIL_W)
            for h in range(DIL_HEADS):
                cols = slice(h * HEAD_DIM, (h + 1) * HEAD_DIM)
                qh = q_ref[rows, cols]
                kc, vc = k_ref[rows, cols], v_ref[rows, cols]
                kp = kp_ref[:, cols] if i == 0 else k_ref[prow, cols]
                vp = vp_ref[:, cols] if i == 0 else v_ref[prow, cols]
                sl = slopes[h] * dilation
                lh = l_ref[rows, h * HEAD_DIM:h * HEAD_DIM + 1]
                sp = lax.dot_general(qh, kp, nt, preferred_element_type=F32) * ATTN_SCALE - sl * rel_prev
                sc = lax.dot_general(qh, kc, nt, preferred_element_type=F32) * ATTN_SCALE - sl * rel_cur
                pp = jnp.exp(jnp.where(valid_prev, sp, NEG_INF) - lh)
                pc = jnp.exp(jnp.where(valid_cur, sc, NEG_INF) - lh)
                doh = do_ref[rows, cols]
                dsum = jnp.sum(doh * o_ref[rows, cols], axis=-1, keepdims=True)
                shift = dl_ref[rows, h * HEAD_DIM:h * HEAD_DIM + 1] - dsum
                dob = doh.astype(_CD)
                dsp = pp * (lax.dot_general(dob, vp, nt, preferred_element_type=F32) + shift)
                dsc = pc * (lax.dot_general(dob, vc, nt, preferred_element_type=F32) + shift)
                dspb = (dsp * ATTN_SCALE).astype(_CD)
                dscb = (dsc * ATTN_SCALE).astype(_CD)
                dq_ref[rows, cols] = (jnp.dot(dspb, kp, preferred_element_type=F32)
                                      + jnp.dot(dscb, kc, preferred_element_type=F32)).astype(dq_ref.dtype)
                dk_s[s_prev, cols] += lax.dot_general(dspb, qh, tn, preferred_element_type=F32)
                dk_s[s_cur, cols] += lax.dot_general(dscb, qh, tn, preferred_element_type=F32)
                dv_s[s_prev, cols] += lax.dot_general(pp.astype(_CD), dob, tn, preferred_element_type=F32)
                dv_s[s_cur, cols] += lax.dot_general(pc.astype(_CD), dob, tn, preferred_element_type=F32)
        dk_ref[...] = dk_s[DIL_W:, :].astype(dk_ref.dtype)
        dv_ref[...] = dv_s[DIL_W:, :].astype(dv_ref.dtype)
        dk_s[CHUNK:, :] = dk_s[:DIL_W, :]
        dv_s[CHUNK:, :] = dv_s[:DIL_W, :]

    cur = pl.BlockSpec((CHUNK, DIL_OUT), lambda s: (nchunk - 1 - s, 0))
    prev = pl.BlockSpec((DIL_W, DIL_OUT), lambda s: (jnp.maximum(4 * (nchunk - 1 - s) - 1, 0), 0))
    shp = jax.ShapeDtypeStruct((S, DIL_OUT), _CD)
    return _pcall(body, name=name, grid=(nchunk,), in_specs=[cur, cur, cur, prev, prev, cur, cur, cur, cur],
                  out_specs=[cur, cur, cur], out_shape=[shp, shp, shp],
                  scratch_shapes=[pltpu.VMEM((CHUNK + DIL_W, DIL_OUT), F32), pltpu.VMEM((CHUNK + DIL_W, DIL_OUT), F32)],
                  compiler_params=_params("arbitrary"))(q, k, v, k, v, o, lse, do, dlse)


def _dil_mix_fwd(os_, ls_, *, name, tm=512):
    S, W = os_[0].shape

    def body(o0, o1, o2, l0, l1, l2, out_ref):
        ls = [l0[...], l1[...], l2[...]]
        m = jnp.maximum(jnp.maximum(ls[0], ls[1]), ls[2])
        es = [jnp.exp(l - m) for l in ls]
        den = es[0] + es[1] + es[2]
        out_ref[...] = ((es[0] * o0[...] + es[1] * o1[...] + es[2] * o2[...]) / den).astype(out_ref.dtype)

    row = pl.BlockSpec((tm, W), lambda i: (i, 0))
    return _pcall(body, name=name, grid=(S // tm,), in_specs=[row] * 6, out_specs=row,
                  out_shape=jax.ShapeDtypeStruct((S, W), _CD), compiler_params=_params("parallel"))(*os_, *ls_)


def _dil_mix_bwd(doa, os_, ls_, *, name, tm=512, after=None):
    S, W = doa.shape

    def body(d_ref, o0, o1, o2, l0, l1, l2, do0, do1, do2, dl0, dl1, dl2):
        dv = d_ref[...]
        ls = [l0[...], l1[...], l2[...]]
        m = jnp.maximum(jnp.maximum(ls[0], ls[1]), ls[2])
        es = [jnp.exp(l - m) for l in ls]
        den = es[0] + es[1] + es[2]
        al = [e / den for e in es]
        da = [_head_sum(dv * o[...]) for o in (o0, o1, o2)]
        mean = al[0] * da[0] + al[1] * da[1] + al[2] * da[2]
        for a, d_, do_ref, dl_ref in zip(al, da, (do0, do1, do2), (dl0, dl1, dl2)):
            do_ref[...] = a * dv
            dl_ref[...] = a * (d_ - mean)

    row = pl.BlockSpec((tm, W), lambda i: (i, 0))
    shp = jax.ShapeDtypeStruct((S, W), F32)
    return _pcall(body, after, name=name, grid=(S // tm,), in_specs=[row] * 7, out_specs=[row] * 6, out_shape=[shp] * 6,
                  compiler_params=_params("parallel"))(doa, *os_, *ls_)


FOX_T = 512


PACK = 2 * HEAD_DIM
HEAD_PAIRS = N_FOX_HEADS // 2
Q_BLOCK0 = (3 * DIL_WIDTH) // PACK
K_BLOCK0 = (3 * DIL_WIDTH + FOX_WIDTH) // PACK
V_BLOCK0 = (3 * DIL_WIDTH + 2 * FOX_WIDTH) // PACK


def _pieces(x):
    hi = x.astype(jnp.bfloat16).astype(F32)
    r = x - hi
    mid = r.astype(jnp.bfloat16).astype(F32)
    lo = (r - mid).astype(jnp.bfloat16).astype(F32)
    return [hi, mid, lo]


def _extras(first, second, rows):
    lane = lax.broadcasted_iota(jnp.int32, (rows, HEAD_DIM), 1)
    out = jnp.zeros((rows, HEAD_DIM), F32)
    for idx, val in enumerate(list(first) + list(second)):
        out = jnp.where(lane == idx, val, out)
    return out


def _head_column(c, h):
    lane = lax.broadcasted_iota(jnp.int32, c.shape, 1)
    return jnp.sum(jnp.where(lane == h, c, 0.0), axis=1, keepdims=True)


ONES3 = [1.0, 1.0, 1.0]
ZEROS3 = [0.0, 0.0, 0.0]


def _fox_pack_fwd(qkv, c, *, name, tm=512):
    S = qkv.shape[0]

    def body(q_ref, k_ref, v_ref, c_ref, qo_ref, ko_ref, vo_ref):
        hp = pl.program_id(1)
        cv = c_ref[...]
        for hh in range(2):
            ch = _pieces(_head_column(cv, 2 * hp + hh))
            src = slice(hh * HEAD_DIM, (hh + 1) * HEAD_DIM)
            lo = slice(hh * PACK, hh * PACK + HEAD_DIM)
            hi = slice(hh * PACK + HEAD_DIM, (hh + 1) * PACK)
            qo_ref[:, lo] = (q_ref[:, src].astype(F32) * ATTN_SCALE).astype(qo_ref.dtype)
            qo_ref[:, hi] = _extras(ch, ONES3, tm).astype(qo_ref.dtype)
            ko_ref[:, lo] = k_ref[:, src]
            ko_ref[:, hi] = _extras(ONES3, [-p for p in ch], tm).astype(ko_ref.dtype)
            vo_ref[:, lo] = v_ref[:, src]
            vo_ref[:, hi] = _extras(ONES3, ZEROS3, tm).astype(vo_ref.dtype)

    def src(block0):
        return pl.BlockSpec((tm, PACK), lambda i, hp: (i, block0 + hp))

    out = pl.BlockSpec((tm, 2 * PACK), lambda i, hp: (i, hp))
    shp = jax.ShapeDtypeStruct((S, N_FOX_HEADS * PACK), _CD)
    return _pcall(body, name=name, grid=(S // tm, HEAD_PAIRS),
                  in_specs=[src(Q_BLOCK0), src(K_BLOCK0), src(V_BLOCK0), pl.BlockSpec((tm, PACK), lambda i, hp: (i, 0))],
                  out_specs=[out, out, out], out_shape=[shp, shp, shp],
                  compiler_params=_params("parallel", "parallel"))(qkv, qkv, qkv, c)


def _fox_fwd(qp, kp, vp, *, name):
    S = qp.shape[0]
    nt = S // FOX_T
    nt_dims = (((1,), (1,)), ((), ()))
    tn_dims = (((0,), (0,)), ((), ()))

    def body(q_ref, k_ref, v_ref, o_ref, l_ref, m_s, acc_s):
        i, j = pl.program_id(1), pl.program_id(2)

        @pl.when(j == 0)
        def _():
            m_s[...] = jnp.full((2, 1, FOX_T), NEG_INF, F32)
            acc_s[...] = jnp.zeros((2, PACK, FOX_T), F32)

        def tile(diagonal):
            for hh in range(2):
                cols = slice(hh * PACK, (hh + 1) * PACK)
                st = lax.dot_general(k_ref[:, cols], q_ref[:, cols], nt_dims, preferred_element_type=F32)
                if diagonal:
                    key = lax.broadcasted_iota(jnp.int32, (FOX_T, FOX_T), 0)
                    qry = lax.broadcasted_iota(jnp.int32, (FOX_T, FOX_T), 1)
                    st = jnp.where(key <= qry, st, NEG_INF)
                m_old = m_s[hh]
                m_new = jnp.maximum(m_old, jnp.max(st, axis=0, keepdims=True))
                pt = jnp.exp(st - m_new)
                acc_s[hh] = jnp.exp(m_old - m_new) * acc_s[hh] + lax.dot_general(
                    v_ref[:, cols], pt.astype(_CD), tn_dims, preferred_element_type=F32)
                m_s[hh] = m_new

        @pl.when(j < i)
        def _():
            tile(False)

        @pl.when(j == i)
        def _():
            tile(True)

        @pl.when(j == nt - 1)
        def _():
            for hh in range(2):
                acc = acc_s[hh]
                den = acc[HEAD_DIM:HEAD_DIM + 1, :]
                cols = slice(hh * HEAD_DIM, (hh + 1) * HEAD_DIM)
                o_ref[:, cols] = (acc[:HEAD_DIM, :] / den).T
                l_ref[:, cols] = jnp.broadcast_to(m_s[hh] + jnp.log(den), (HEAD_DIM, FOX_T)).T

    qs = pl.BlockSpec((FOX_T, 2 * PACK), lambda hp, i, j: (i, hp))
    ks = pl.BlockSpec((FOX_T, 2 * PACK), lambda hp, i, j: (jnp.minimum(i, j), hp))
    os_ = pl.BlockSpec((FOX_T, PACK), lambda hp, i, j: (i, hp))
    shp = jax.ShapeDtypeStruct((S, FOX_WIDTH), F32)
    return _pcall(body, name=name, grid=(HEAD_PAIRS, nt, nt), in_specs=[qs, ks, ks], out_specs=[os_, os_],
                  out_shape=[shp, shp],
                  scratch_shapes=[pltpu.VMEM((2, 1, FOX_T), F32), pltpu.VMEM((2, PACK, FOX_T), F32)],
                  compiler_params=_params("parallel", "parallel", "arbitrary"))(qp, kp, vp)


def _fox_pack_bwd(qkv, c, o, lse, do, *, name, tm=512, after=None):
    S = qkv.shape[0]

    def body(q_ref, c_ref, o_ref, l_ref, do_ref, qo_ref, do_out_ref):
        hp = pl.program_id(1)
        cv = c_ref[...]
        for hh in range(2):
            src = slice(hh * HEAD_DIM, (hh + 1) * HEAD_DIM)
            lo = slice(hh * PACK, hh * PACK + HEAD_DIM)
            hi = slice(hh * PACK + HEAD_DIM, (hh + 1) * PACK)
            shift = _head_column(cv, 2 * hp + hh) - l_ref[:, hh * HEAD_DIM:hh * HEAD_DIM + 1]
            dov = do_ref[:, src]
            dsum = jnp.sum(dov * o_ref[:, src], axis=-1, keepdims=True)
            qo_ref[:, lo] = (q_ref[:, src].astype(F32) * ATTN_SCALE).astype(qo_ref.dtype)
            qo_ref[:, hi] = _extras(_pieces(shift), ONES3, tm).astype(qo_ref.dtype)
            do_out_ref[:, lo] = dov.astype(do_out_ref.dtype)
            do_out_ref[:, hi] = _extras(_pieces(-dsum), ZEROS3, tm).astype(do_out_ref.dtype)

    pair = pl.BlockSpec((tm, PACK), lambda i, hp: (i, hp))
    out = pl.BlockSpec((tm, 2 * PACK), lambda i, hp: (i, hp))
    shp = jax.ShapeDtypeStruct((S, N_FOX_HEADS * PACK), _CD)
    return _pcall(body, after, name=name, grid=(S // tm, HEAD_PAIRS),
                  in_specs=[pl.BlockSpec((tm, PACK), lambda i, hp: (i, Q_BLOCK0 + hp)),
                            pl.BlockSpec((tm, PACK), lambda i, hp: (i, 0)), pair, pair, pair],
                  out_specs=[out, out], out_shape=[shp, shp],
                  compiler_params=_params("parallel", "parallel"))(qkv, c, o, lse, do)


def _fox_bwd(qp, kp, vp, dop, *, name):
    S = qp.shape[0]
    nt = S // FOX_T
    nt_dims = (((1,), (1,)), ((), ()))
    tn_dims = (((0,), (0,)), ((), ()))

    def body(q_ref, k_ref, v_ref, do_ref, dq_ref, dk_ref, dv_ref, dc_ref, dr_ref, dq_s, dk_s, dv_s, dc_s, dr_s):
        j, i = pl.program_id(1), pl.program_id(2)

        @pl.when(jnp.logical_and(j == 0, i == 0))
        def _():
            dq_s[...] = jnp.zeros((S, 2 * PACK), F32)
            dr_s[...] = jnp.zeros((2, 1, S), F32)

        @pl.when(i == 0)
        def _():
            dk_s[...] = jnp.zeros((FOX_T, 2 * PACK), F32)
            dv_s[...] = jnp.zeros((FOX_T, 2 * PACK), F32)
            dc_s[...] = jnp.zeros((2, FOX_T, 1), F32)

        def tile(diagonal):
            rows = pl.ds(pl.multiple_of(i * FOX_T, FOX_T), FOX_T)
            for hh in range(2):
                cols = slice(hh * PACK, (hh + 1) * PACK)
                qv, kv, vv, dov = q_ref[:, cols], k_ref[:, cols], v_ref[:, cols], do_ref[:, cols]
                pt = jnp.exp(lax.dot_general(kv, qv, nt_dims, preferred_element_type=F32))
                if diagonal:
                    key = lax.broadcasted_iota(jnp.int32, (FOX_T, FOX_T), 0)
                    qry = lax.broadcasted_iota(jnp.int32, (FOX_T, FOX_T), 1)
                    pt = jnp.where(key <= qry, pt, 0.0)
                dst = pt * lax.dot_general(vv, dov, nt_dims, preferred_element_type=F32)
                dsb = dst.astype(_CD)
                dc_s[hh] += jnp.sum(dst, axis=1, keepdims=True)
                dr_s[hh, :, rows] += jnp.sum(dst, axis=0, keepdims=True)
                dv_s[:, cols] += jnp.dot(pt.astype(_CD), dov, preferred_element_type=F32)
                dk_s[:, cols] += jnp.dot(dsb, qv, preferred_element_type=F32)
                dq_s[rows, cols] += lax.dot_general(dsb, kv, tn_dims, preferred_element_type=F32)

        @pl.when(i > j)
        def _():
            tile(False)

        @pl.when(i == j)
        def _():
            tile(True)

        @pl.when(i == nt - 1)
        def _():
            dk_ref[...] = dk_s[...].astype(dk_ref.dtype)
            dv_ref[...] = dv_s[...].astype(dv_ref.dtype)
            for hh in range(2):
                dc_ref[:, hh * HEAD_DIM:(hh + 1) * HEAD_DIM] = jnp.broadcast_to(dc_s[hh], (FOX_T, HEAD_DIM))

        @pl.when(jnp.logical_and(j == nt - 1, i == nt - 1))
        def _():
            lane = lax.broadcasted_iota(jnp.int32, (S, 2 * PACK), 1) % PACK
            dq_ref[...] = (dq_s[...] * jnp.where(lane < HEAD_DIM, ATTN_SCALE, 1.0)).astype(dq_ref.dtype)
            dr_ref[...] = dr_s[...]

    qs = pl.BlockSpec((FOX_T, 2 * PACK), lambda hp, j, i: (jnp.maximum(i, j), hp))
    ks = pl.BlockSpec((FOX_T, 2 * PACK), lambda hp, j, i: (j, hp))
    whole = pl.BlockSpec((S, 2 * PACK), lambda hp, j, i: (0, hp))
    cs = pl.BlockSpec((FOX_T, PACK), lambda hp, j, i: (j, hp))
    rs = pl.BlockSpec((2, 1, S), lambda hp, j, i: (hp, 0, 0))
    shp = jax.ShapeDtypeStruct((S, N_FOX_HEADS * PACK), _CD)
    return _pcall(body, name=name, grid=(HEAD_PAIRS, nt, nt), in_specs=[qs, ks, ks, qs],
                  out_specs=[whole, ks, ks, cs, rs],
                  out_shape=[shp, shp, shp, jax.ShapeDtypeStruct((S, FOX_WIDTH), F32),
                             jax.ShapeDtypeStruct((N_FOX_HEADS, 1, S), F32)],
                  scratch_shapes=[pltpu.VMEM((S, 2 * PACK), F32), pltpu.VMEM((FOX_T, 2 * PACK), F32),
                                  pltpu.VMEM((FOX_T, 2 * PACK), F32), pltpu.VMEM((2, FOX_T, 1), F32),
                                  pltpu.VMEM((2, 1, S), F32)],
                  compiler_params=_params("parallel", "arbitrary", "arbitrary"))(qp, kp, vp, dop)


def _dedilate(t, d):
    if d == 1:
        return t
    S, C = t.shape
    return t.reshape(S // d, d, C).transpose(1, 0, 2).reshape(S, C)


def _redilate(t, d):
    if d == 1:
        return t
    S, C = t.shape
    return t.reshape(d, S // d, C).transpose(1, 0, 2).reshape(S, C)


def _layer_step(x, tgt, w, p, late_weights=None, grad_sink=None, after=None):
    S = x.shape[0]
    h = _rms_fwd(x, p["norm_mix_g"], name="rms_mix", after=after)
    qkv = _mm(h, w["qkv"], name="proj_qkv", out_dtype=_CD, tn=768)
    zf = _mm(h, w["f"], name="proj_f")
    gl = _mm(h, w["g"], name="proj_gate", tn=1024)

    dil_q, dil_k, dil_v = [], [], []
    dil_o, dil_l = [], []
    for g, (_, d) in enumerate(DIL_PAIRS):
        qg = _dedilate(qkv[:, g * DIL_OUT:(g + 1) * DIL_OUT], d)
        kg = _dedilate(qkv[:, DIL_WIDTH + g * DIL_OUT:DIL_WIDTH + (g + 1) * DIL_OUT], d)
        vg = _dedilate(qkv[:, 2 * DIL_WIDTH + g * DIL_OUT:2 * DIL_WIDTH + (g + 1) * DIL_OUT], d)
        og, lg = _dil_fwd(qg, kg, vg, g, name=f"dil_fwd{g}")
        dil_q.append(qg), dil_k.append(kg), dil_v.append(vg)
        dil_o.append(_redilate(og, d)), dil_l.append(_redilate(lg, d))
    o_a = _dil_mix_fwd(dil_o, dil_l, name="dil_mix")

    c = _fox_cumsum(zf, p["b_fgt"], name="fox_cumsum")
    fqp, fkp, fvp = _fox_pack_fwd(qkv, c, name="fox_pack")
    o_b, flse = _fox_fwd(fqp, fkp, fvp, name="fox_fwd")

    if late_weights is not None:
        w = {**w, **late_weights(o_b)}
    y_a = _mm(o_a, w["dil_out"], name="y_a", tn=1024)
    y_b = _mm(o_b, w["fox_out"], name="y_b", tn=1024)
    merged = _gate_fwd(gl, p["b_gate"], y_a, y_b, name="gate_fwd")
    x1 = _mm(merged, w["out"], name="mix_out", add=x)

    h2 = _rms_fwd(x1, p["norm_ffn_g"], name="rms_ffn")
    gu = _mm(h2, w["ffn_in"], name="ffn_in", tn=1408)
    act = _swiglu_fwd(gu, name="swiglu")
    x2 = _mm(act, w["ffn_down"], name="ffn_down", add=x1, tk=2816)

    loss, dx2, dg_final = _loss_head(x2, p["norm_final_g"], tgt, name="loss_head")

    dact = _mm(dx2, w["ffn_down"], name="d_act", tb=True, tn=1408)
    gw_ffn_down = _mm(act, dx2, name="gw_ffn_down", ta=True, out_dtype=_CD, tm=1408)
    dgu = _swiglu_bwd(dact, gu, name="swiglu_bwd")
    dh2 = _mm(dgu, w["ffn_in"], name="d_h2", tb=True, tk=2816)
    gw_ffn_in = _mm(h2, dgu, name="gw_ffn_in", ta=True, out_dtype=_CD)
    sink = grad_sink if grad_sink is not None else (lambda group, grads: None)
    tok = sink("ffn", dict(ffn_in=gw_ffn_in, ffn_down=gw_ffn_down))
    dx1, dg_ffn = _rms_bwd(x1, p["norm_ffn_g"], dh2, dx2, name="rms_ffn_bwd", after=tok)

    dmerged = _mm(dx1, w["out"], name="d_merged", tb=True)
    gw_out = _mm(merged, dx1, name="gw_out", ta=True, out_dtype=_CD)
    dy_a, dy_b, dgl, db_gate = _gate_bwd(dmerged, gl, p["b_gate"], y_a, y_b, name="gate_bwd")
    do_a = _mm(dy_a, w["dil_out"], name="d_o_a", tb=True)
    gw_dil_out = _mm(o_a, dy_a, name="gw_dil_out", ta=True, out_dtype=_CD, tn=1024)
    do_b = _mm(dy_b, w["fox_out"], name="d_o_b", tb=True)
    gw_fox_out = _mm(o_b, dy_b, name="gw_fox_out", ta=True, out_dtype=_CD, tn=1024)
    tok = sink("mix", dict(dil_out=gw_dil_out, fox_out=gw_fox_out, out=gw_out))

    bqp, bdop = _fox_pack_bwd(qkv, c, o_b, flse, do_b, name="fox_pack_bwd", after=tok)
    dqp, dkp, dvp, dck, dcq = _fox_bwd(bqp, fkp, fvp, bdop, name="fox_bwd")
    dc = dcq[:, 0, :].T - dck.reshape(S, N_FOX_HEADS, HEAD_DIM)[:, :, 0]
    dc = jnp.pad(dc, ((0, 0), (0, F_PAD - N_FOX_HEADS)))
    dzf, db_fgt = _fox_cumsum_bwd(dc, zf, p["b_fgt"], name="fox_cumsum_bwd")

    def unpack(t):
        return t.reshape(S, N_FOX_HEADS, PACK)[:, :, :HEAD_DIM].reshape(S, FOX_WIDTH)

    douts = _dil_mix_bwd(do_a, dil_o, dil_l, name="dil_mix_bwd", after=tok)
    dqs, dks, dvs = [], [], []
    for g, (_, d) in enumerate(DIL_PAIRS):
        dq, dk, dv = _dil_bwd(dil_q[g], dil_k[g], dil_v[g], _dedilate(dil_o[g], d), _dedilate(dil_l[g], d),
                              _dedilate(douts[g], d), _dedilate(douts[3 + g], d), g, name=f"dil_bwd{g}")
        dqs.append(_redilate(dq, d)), dks.append(_redilate(dk, d)), dvs.append(_redilate(dv, d))
    dqkv = jnp.concatenate(dqs + dks + dvs + [unpack(dqp), unpack(dkp), unpack(dvp)], axis=1)

    gw_qkv = _mm(h, dqkv, name="gw_qkv", ta=True, out_dtype=_CD, tn=768)
    gw_g = _mm(h, dgl, name="gw_gate", ta=True, out_dtype=_CD)
    gw_f = _mm(h, dzf, name="gw_f", ta=True, out_dtype=_CD)
    tok = sink("in", dict(qkv=gw_qkv, f=gw_f, g=gw_g))
    dh = _mm(dqkv, w["qkv"], name="d_h_qkv", tb=True, tk=1920, after=tok)
    dh = _mm(dgl, w["g"], name="d_h_gate", tb=True, add=dh)
    dh = _mm(dzf, w["f"], name="d_h_f", tb=True, add=dh)
    dx, dg_mix = _rms_bwd(x, p["norm_mix_g"], dh, dx1, name="rms_mix_bwd")

    gw = dict(qkv=gw_qkv, f=gw_f, g=gw_g, dil_out=gw_dil_out, fox_out=gw_fox_out, out=gw_out, ffn_in=gw_ffn_in,
              ffn_down=gw_ffn_down)
    small = dict(norm_mix_g=dg_mix, b_fgt=db_fgt, b_gate=db_gate, norm_ffn_g=dg_ffn, norm_final_g=dg_final)
    return loss, dx, gw, small


def _position():
    return lax.axis_index("x"), lax.axis_index("y"), lax.axis_index("c")


def _other_chips(x, y):
    return [(1 - x, y), (x, 1 - y), (1 - x, 1 - y)]


ROW_TILE = 16


def _row_chunks(rows, want=4):
    n = want
    while n > 1 and rows % (n * ROW_TILE):
        n //= 2
    return n


def _gather_weights(shards):
    n = len(shards)
    nq = max(_row_chunks(s.shape[0] // 2) for s in shards)

    def body(*refs):
        ins, outs = refs[:n], refs[n:2 * n]
        send_sems, recv_sems = refs[2 * n:]
        x, y, c = _position()
        mine = 2 * x + y
        chips = _other_chips(x, y)
        started = []

        def pieces(w, core):
            half = ins[w].shape[0] // 2
            size = half // _row_chunks(half)
            return [pl.ds(core * half + q * size, size) for q in range(_row_chunks(half))]

        for w in range(n):
            for r, (cx, cy) in enumerate(chips):
                for q, rows in enumerate(pieces(w, c)):
                    cp = pltpu.make_async_remote_copy(
                        src_ref=ins[w].at[rows, :], dst_ref=outs[w].at[mine, rows, :], send_sem=send_sems.at[w, r, q],
                        recv_sem=recv_sems.at[w, r, q], device_id=(cx, cy, c), device_id_type=MESH)
                    cp.start()
                    started.append(cp)

        def landed(w, r, q, rows, peer):
            cx, cy = chips[r % 3]
            blk = outs[w].at[2 * cx + cy, rows, :]
            return pltpu.make_async_remote_copy(src_ref=blk, dst_ref=blk, send_sem=send_sems.at[w, r, q],
                                                recv_sem=recv_sems.at[w, r, q], device_id=peer, device_id_type=MESH)

        for w in range(n):
            for r, (cx, cy) in enumerate(chips):
                for q, rows in enumerate(pieces(w, c)):
                    landed(w, r, q, rows, (cx, cy, c)).wait_recv()
                    fwd = landed(w, 3 + r, q, rows, (x, y, 1 - c))
                    fwd.start()
                    started.append(fwd)
        for w in range(n):
            for r in range(3):
                for q, rows in enumerate(pieces(w, 1 - c)):
                    landed(w, 3 + r, q, rows, (x, y, 1 - c)).wait_recv()
        for cp in started:
            cp.wait_send()

    return _pcall(
        body, name="gather_weights", in_specs=[HBM_SPEC] * n, out_specs=[HBM_SPEC] * n,
        out_shape=[jax.ShapeDtypeStruct((4,) + s.shape, s.dtype) for s in shards],
        scratch_shapes=[pltpu.SemaphoreType.DMA((n, 6, nq)), pltpu.SemaphoreType.DMA((n, 6, nq))],
    )(*shards)


SEM_SPEC = pl.BlockSpec(memory_space=pltpu.SEMAPHORE)
ANY_SPEC = pl.BlockSpec(memory_space=pl.ANY)
DATAFLOW = pltpu.SideEffectType.DATAFLOW_SIDE_EFFECTING


def _in_hbm(a):
    return pltpu.with_memory_space_constraint(a, pltpu.HBM)


def _split_copy_start(srcs, land_shapes, copies, after, *, name):
    n, m = len(srcs), len(land_shapes)

    def body(*refs):
        src_refs, land_refs = refs[:n], refs[n:n + m]
        send_sems, recv_sems = refs[n + m + 1], refs[n + m + 2]
        token = refs[-1]
        x, y, c = _position()
        for k, (src, dst, peer) in enumerate(copies(x, y, c, src_refs, land_refs)):
            pltpu.make_async_remote_copy(src_ref=src, dst_ref=dst, send_sem=send_sems.at[k], recv_sem=recv_sems.at[k],
                                         device_id=peer, device_id_type=MESH).start()
        token[...] = jnp.zeros_like(token)

    lands = [lax.empty(s.shape, s.dtype) for s in land_shapes]
    count = len(copies(0, 0, 0, srcs, lands))
    out = _pcall(
        body, name=name,
        out_shape=(pltpu.SemaphoreType.DMA((count,)), pltpu.SemaphoreType.DMA((count,)),
                   *[pltpu.HBM(s.shape, s.dtype) for s in srcs], *[pltpu.HBM(s.shape, s.dtype) for s in land_shapes],
                   jax.ShapeDtypeStruct((8, 128), F32)),
        in_specs=[HBM_SPEC] * (n + m) + [ANY_SPEC],
        out_specs=(SEM_SPEC, SEM_SPEC, *[HBM_SPEC] * (n + m), pl.BlockSpec(memory_space=pltpu.VMEM)),
        input_output_aliases={k: 2 + k for k in range(n + m)},
        compiler_params=pltpu.CompilerParams(has_side_effects=DATAFLOW),
    )(*[_in_hbm(s) for s in srcs], *[_in_hbm(l) for l in lands], after)
    return out[0], out[1], list(out[2:2 + n]), list(out[2 + n:2 + n + m]), out[-1]


def _split_copy_wait(send_sems, recv_sems, srcs, lands, copies, after, *, name):
    n, m = len(srcs), len(lands)

    def body(*refs):
        src_refs, land_refs = refs[:n], refs[n:n + m]
        send, recv = refs[n + m], refs[n + m + 1]
        x, y, c = _position()
        for k, (src, dst, peer) in enumerate(copies(x, y, c, src_refs, land_refs)):
            cp = pltpu.make_async_remote_copy(src_ref=src, dst_ref=dst, send_sem=send.at[k], recv_sem=recv.at[k],
                                              device_id=peer, device_id_type=MESH)
            cp.wait_send()
            cp.wait_recv()

    out = _pcall(
        body, name=name,
        out_shape=tuple(pltpu.HBM(s.shape, s.dtype) for s in list(srcs) + list(lands)),
        in_specs=[HBM_SPEC] * (n + m) + [SEM_SPEC, SEM_SPEC, ANY_SPEC], out_specs=tuple([HBM_SPEC] * (n + m)),
        input_output_aliases={k: k for k in range(n + m)},
        compiler_params=pltpu.CompilerParams(has_side_effects=DATAFLOW),
    )(*srcs, *lands, send_sems, recv_sems, after)
    return list(out[:n]), list(out[n:])


def _gather_copies(x, y, c, shard_refs, land_refs):
    out = []
    for s, l in zip(shard_refs, land_refs):
        half = s.shape[0] // 2
        nq = _row_chunks(half)
        for cx, cy in _other_chips(x, y):
            for q in range(nq):
                rows = pl.ds(c * half + q * (half // nq), half // nq)
                out.append((s.at[rows, :], l.at[2 * x + y, rows, :], (cx, cy, c)))
    return out


def _scatter_copies(x, y, c, part_refs, land_refs):
    out = []
    for p, l in zip(part_refs, land_refs):
        nq = _row_chunks(p.shape[1])
        for r, (cx, cy) in enumerate(_other_chips(x, y)):
            for q in range(nq):
                rows = pl.ds(q * (p.shape[1] // nq), p.shape[1] // nq)
                out.append((p.at[2 * cx + cy, rows, :], l.at[r, rows, :], (cx, cy, c)))
    return out


def _forward_halves(lands, *, name):
    n = len(lands)

    def body(*refs):
        ins = refs[:n]
        send_sems, recv_sems = refs[2 * n:]
        x, y, c = _position()
        copies = []
        for w in range(n):
            half = ins[w].shape[1] // 2
            for r, (cx, cy) in enumerate(_other_chips(x, y)):
                blk = ins[w].at[2 * cx + cy, pl.ds(c * half, half), :]
                cp = pltpu.make_async_remote_copy(src_ref=blk, dst_ref=blk, send_sem=send_sems.at[w, r],
                                                  recv_sem=recv_sems.at[w, r], device_id=(x, y, 1 - c),
                                                  device_id_type=MESH)
                cp.start()
                copies.append(cp)
        for w in range(n):
            half = ins[w].shape[1] // 2
            for r, (cx, cy) in enumerate(_other_chips(x, y)):
                blk = ins[w].at[2 * cx + cy, pl.ds((1 - c) * half, half), :]
                pltpu.make_async_remote_copy(src_ref=blk, dst_ref=blk, send_sem=send_sems.at[w, r],
                                             recv_sem=recv_sems.at[w, r], device_id=(x, y, 1 - c),
                                             device_id_type=MESH).wait_recv()
        for cp in copies:
            cp.wait_send()

    return _pcall(
        body, name=name, in_specs=[HBM_SPEC] * n, out_specs=[HBM_SPEC] * n,
        out_shape=[jax.ShapeDtypeStruct(l.shape, l.dtype) for l in lands],
        input_output_aliases={k: k for k in range(n)},
        scratch_shapes=[pltpu.SemaphoreType.DMA((n, 3)), pltpu.SemaphoreType.DMA((n, 3))],
    )(*lands)


def _swap_halves(grads, name="swap_halves"):
    n = len(grads)

    def body(*refs):
        ins, outs = refs[:n], refs[n:2 * n]
        send_sems, recv_sems = refs[2 * n:]
        x, y, c = _position()
        copies = []
        for w in range(n):
            half = ins[w].shape[1] // 2
            cp = pltpu.make_async_remote_copy(
                src_ref=ins[w].at[:, pl.ds((1 - c) * half, half), :], dst_ref=outs[w], send_sem=send_sems.at[w],
                recv_sem=recv_sems.at[w], device_id=(x, y, 1 - c), device_id_type=MESH)
            cp.start()
            copies.append(cp)
        for cp in copies:
            cp.wait()

    return _pcall(
        body, name=name, in_specs=[HBM_SPEC] * n, out_specs=[HBM_SPEC] * n,
        out_shape=[jax.ShapeDtypeStruct((4, g.shape[1] // 2, g.shape[2]), g.dtype) for g in grads],
        scratch_shapes=[pltpu.SemaphoreType.DMA((n,)), pltpu.SemaphoreType.DMA((n,))],
    )(*grads)


def _share_halves(halves):
    n = len(halves)

    def body(*refs):
        ins, outs = refs[:n], refs[n:2 * n]
        send_sems, recv_sems = refs[2 * n:]
        x, y, c = _position()
        copies = []
        for w in range(n):
            cp = pltpu.make_async_remote_copy(src_ref=ins[w], dst_ref=outs[w], send_sem=send_sems.at[w],
                                              recv_sem=recv_sems.at[w], device_id=(x, y, 1 - c), device_id_type=MESH)
            cp.start()
            copies.append(cp)
        for cp in copies:
            cp.wait()

    return _pcall(
        body, name="share_halves", in_specs=[HBM_SPEC] * n, out_specs=[HBM_SPEC] * n,
        out_shape=[jax.ShapeDtypeStruct(h.shape, h.dtype) for h in halves],
        scratch_shapes=[pltpu.SemaphoreType.DMA((n,)), pltpu.SemaphoreType.DMA((n,))],
    )(*halves)


def _sum_small(part):
    rows, width = part.shape

    def body(x_ref, out_ref, all_ref, send_sems, recv_sems):
        x, y, c = _position()
        me, sibling = (x, y, c), (x, y, 1 - c)
        chips = _other_chips(x, y)

        def block(px, py, pc):
            return all_ref.at[pl.ds((4 * px + 2 * py + pc) * rows, rows), :]

        def copy(k, blk, to, src=None):
            return pltpu.make_async_remote_copy(
                src_ref=block(*blk) if src is None else src, dst_ref=block(*blk), send_sem=send_sems.at[k],
                recv_sem=recv_sems.at[k], device_id=to, device_id_type=MESH)

        all_ref[pl.ds((4 * x + 2 * y + c) * rows, rows), :] = x_ref[...]
        first = [copy(0, me, sibling, src=x_ref)]
        first += [copy(1 + j, me, (*chip, c), src=x_ref) for j, chip in enumerate(chips)]
        for cp in first:
            cp.start()
        passed = [copy(4 + j, (*chip, c), sibling) for j, chip in enumerate(chips)]
        for j, chip in enumerate(chips):
            copy(1 + j, (*chip, c), me).wait_recv()
            passed[j].start()
        copy(0, sibling, me).wait_recv()
        for j, chip in enumerate(chips):
            copy(4 + j, (*chip, 1 - c), me).wait_recv()
        for cp in first + passed:
            cp.wait_send()
        total = all_ref[0:rows, :]
        for d in range(1, 8):
            total = total + all_ref[d * rows:(d + 1) * rows, :]
        out_ref[...] = total

    vm = pl.BlockSpec(memory_space=pltpu.VMEM)
    return _pcall(
        body, name="sum_small", in_specs=[vm], out_specs=vm, out_shape=jax.ShapeDtypeStruct((rows, width), F32),
        scratch_shapes=[pltpu.VMEM((8 * rows, width), F32), pltpu.SemaphoreType.DMA((7,)), pltpu.SemaphoreType.DMA((7,))],
    )(part)


def _row_tile(R, C, itemsize=4, budget=1 << 20):
    for t in (512, 256, 128, 64, 32, 16, 8):
        if R % t == 0 and t * C * itemsize <= budget:
            return t
    return R


def _add_halves(g, recv, c, *, name):
    _, R, C = g.shape
    half = R // 2
    t = _row_tile(half, C)
    nb = half // t

    def body(c_ref, g_ref, r_ref, o_ref):
        o_ref[...] = (g_ref[...].astype(F32) + r_ref[...].astype(F32)).astype(o_ref.dtype)

    grid_spec = pltpu.PrefetchScalarGridSpec(
        num_scalar_prefetch=1, grid=(4, nb),
        in_specs=[pl.BlockSpec((1, t, C), lambda k, i, cr: (k, cr[0] * nb + i, 0)),
                  pl.BlockSpec((1, t, C), lambda k, i, cr: (k, i, 0))],
        out_specs=pl.BlockSpec((1, t, C), lambda k, i, cr: (k, i, 0)))
    return _pcall(body, name=name, grid_spec=grid_spec, out_shape=jax.ShapeDtypeStruct((4, half, C), g.dtype),
                  compiler_params=_params("parallel", "parallel"))(c, g, recv)


def _add_owners(mine, recv, *, name):
    half, C = mine.shape
    t = _row_tile(half, C)

    def body(m_ref, r_ref, o_ref):
        o_ref[...] = ((m_ref[...].astype(F32) + r_ref[0].astype(F32)) + r_ref[1].astype(F32)) + r_ref[2].astype(F32)

    return _pcall(body, name=name, grid=(half // t,),
                  in_specs=[pl.BlockSpec((t, C), lambda i: (i, 0)), pl.BlockSpec((3, t, C), lambda i: (0, i, 0))],
                  out_specs=pl.BlockSpec((t, C), lambda i: (i, 0)), out_shape=jax.ShapeDtypeStruct((half, C), F32),
                  compiler_params=_params("parallel"))(mine, recv)


def _adamw(w, g, m, v, *, name):
    R, C = w.shape
    t = _row_tile(R, C)
    c1 = 1.0 - ADAM_B1 ** ADAM_STEP
    c2 = 1.0 - ADAM_B2 ** ADAM_STEP

    def body(w_ref, g_ref, m_ref, v_ref, d_ref, nm_ref, nv_ref):
        gv = g_ref[...]
        mn = ADAM_B1 * m_ref[...] + (1.0 - ADAM_B1) * gv
        vn = ADAM_B2 * v_ref[...] + (1.0 - ADAM_B2) * (gv * gv)
        d_ref[...] = -ADAM_LR * ((mn / c1) / (jnp.sqrt(vn / c2) + ADAM_EPS) + ADAM_WD * w_ref[...])
        nm_ref[...] = mn
        nv_ref[...] = vn

    blk = pl.BlockSpec((t, C), lambda i: (i, 0))
    shp = jax.ShapeDtypeStruct((R, C), F32)
    return _pcall(body, name=name, grid=(R // t,), in_specs=[blk] * 4, out_specs=[blk] * 3, out_shape=[shp] * 3,
                  compiler_params=_params("parallel"))(w, g, m, v)


BIG = ("w_in", "w_dil_out", "w_fox_out", "w_out", "w_ffn_in", "w_ffn_down")
SMALL = ("norm_mix_g", "b_fgt", "b_gate", "norm_ffn_g", "norm_final_g")
ORDER = ("norm_mix_g", "w_in", "b_fgt", "b_gate", "w_dil_out", "w_fox_out", "w_out", "norm_ffn_g", "w_ffn_in",
         "w_ffn_down", "norm_final_g")
SMALL_ROWS = {"norm_mix_g": (0, 1), "b_gate": (1, 3), "norm_ffn_g": (3, 4), "norm_final_g": (4, 5), "b_fgt": (5, 6)}


def _columns_to_blocks(full, ncol):
    K = full.shape[0]
    return full.reshape(K, 4, ncol).transpose(1, 0, 2)


def _blocks_to_columns(blocks):
    n, K, ncol = blocks.shape
    return blocks.transpose(1, 0, 2).reshape(K, n * ncol)


def kernel(x, norm_mix_g, w_in, b_fgt, b_gate, w_dil_out, w_fox_out, w_out, norm_ffn_g, w_ffn_in, w_ffn_down, norm_final_g, loss_target, m_norm_mix_g, m_w_in, m_b_fgt, m_b_gate, m_w_dil_out, m_w_fox_out, m_w_out, m_norm_ffn_g, m_w_ffn_in, m_w_ffn_down, m_norm_final_g, v_norm_mix_g, v_w_in, v_b_fgt, v_b_gate, v_w_dil_out, v_w_fox_out, v_w_out, v_norm_ffn_g, v_w_ffn_in, v_w_ffn_down, v_norm_final_g):
    weights = dict(norm_mix_g=norm_mix_g, w_in=w_in, b_fgt=b_fgt, b_gate=b_gate, w_dil_out=w_dil_out,
                   w_fox_out=w_fox_out, w_out=w_out, norm_ffn_g=norm_ffn_g, w_ffn_in=w_ffn_in, w_ffn_down=w_ffn_down,
                   norm_final_g=norm_final_g)
    m_in = dict(norm_mix_g=m_norm_mix_g, w_in=m_w_in, b_fgt=m_b_fgt, b_gate=m_b_gate, w_dil_out=m_w_dil_out,
                w_fox_out=m_w_fox_out, w_out=m_w_out, norm_ffn_g=m_norm_ffn_g, w_ffn_in=m_w_ffn_in,
                w_ffn_down=m_w_ffn_down, norm_final_g=m_norm_final_g)
    v_in = dict(norm_mix_g=v_norm_mix_g, w_in=v_w_in, b_fgt=v_b_fgt, b_gate=v_b_gate, w_dil_out=v_w_dil_out,
                w_fox_out=v_w_fox_out, w_out=v_w_out, norm_ffn_g=v_norm_ffn_g, w_ffn_in=v_w_ffn_in,
                w_ffn_down=v_w_ffn_down, norm_final_g=v_norm_final_g)
    c = lax.axis_index("c")
    chip = 2 * lax.axis_index("x") + lax.axis_index("y")

    shards = {n: weights[n][0].astype(_CD) for n in BIG}
    (g_in,) = _gather_weights([shards["w_in"]])
    late = BIG[1:]
    send_g, recv_g, late_src, late_land, token = _split_copy_start(
        [shards[n] for n in late], [jax.ShapeDtypeStruct((4,) + shards[n].shape, _CD) for n in late],
        _gather_copies, g_in, name="gather_late_start")
    full_in = _blocks_to_columns(lax.dynamic_update_index_in_dim(g_in, shards["w_in"], chip, 0))
    o3 = QKV_COLS
    o4 = o3 + N_FOX_HEADS
    w = dict(qkv=full_in[:, :o3], f=jnp.pad(full_in[:, o3:o4], ((0, 0), (0, F_PAD - N_FOX_HEADS))), g=full_in[:, o4:])
    p = dict(norm_mix_g=norm_mix_g, b_fgt=jnp.pad(b_fgt, ((0, 0), (0, F_PAD - N_FOX_HEADS))), b_gate=b_gate,
             norm_ffn_g=norm_ffn_g, norm_final_g=norm_final_g.reshape(1, D_MODEL))

    def late_weights(after):
        own, lands = _split_copy_wait(send_g, recv_g, late_src, late_land, _gather_copies, after,
                                      name="gather_late_wait")
        lands = _forward_halves(lands, name="gather_late_forward")
        g_dil, g_fox, g_out, g_ffn_in, g_ffn_down = [
            lax.dynamic_update_index_in_dim(l, s, chip, 0) for l, s in zip(lands, own)]
        return dict(dil_out=_blocks_to_columns(g_dil), fox_out=_blocks_to_columns(g_fox),
                    out=g_out.reshape(D_MODEL, D_MODEL), ffn_in=_blocks_to_columns(g_ffn_in),
                    ffn_down=g_ffn_down.reshape(D_FF, D_MODEL))

    c_arr = jnp.reshape(c, (1,)).astype(jnp.int32)

    def to_blocks(n, full):
        shape = weights[n].shape
        if n in ("w_out", "w_ffn_down"):
            return full.reshape(4, shape[1], shape[2])
        return _columns_to_blocks(full, shape[2])

    def pair_sums(group, named):
        names = list(named)
        blocks = [to_blocks(n, named[n]) for n in names]
        from_sibling = _swap_halves(blocks, name=f"swap_halves_{group}")
        return [_add_halves(b, r, c_arr, name=f"add_halves_{n}") for b, r, n in zip(blocks, from_sibling, names)]

    in_flight = {}

    def grad_sink(group, gw):
        if group == "in":
            named = {"w_in": jnp.concatenate([gw["qkv"], gw["f"][:, :N_FOX_HEADS], gw["g"]], axis=1)}
        else:
            named = {"w_" + k: v for k, v in gw.items()}
        sums = pair_sums(group, named)
        started = _split_copy_start(sums, [jax.ShapeDtypeStruct((3,) + s.shape[1:], s.dtype) for s in sums],
                                    _scatter_copies, next(iter(gw.values())), name=f"scatter_{group}_start")
        in_flight[group] = (list(named), started)
        return started[-1]

    loss_part, grad_x, gw, small = _layer_step(x[0], loss_target[0], w, p, late_weights, grad_sink, token)

    def owner_sums(names, sums, from_chips):
        return {n: _add_owners(lax.dynamic_index_in_dim(s, chip, 0, keepdims=False), r, name=f"add_owners_{n}")
                for n, s, r in zip(names, sums, from_chips)}

    halves = {}
    for group, (names, (send_s, recv_s, srcs, lands, _)) in in_flight.items():
        sums, from_chips = _split_copy_wait(send_s, recv_s, srcs, lands, _scatter_copies, grad_x,
                                            name=f"scatter_{group}_wait")
        halves.update(owner_sums(names, sums, from_chips))
    halves = [halves[n] for n in BIG]
    grads = {}
    for n, own, other in zip(BIG, halves, _share_halves(halves)):
        pair = jnp.stack([own, other])
        grads[n] = jnp.where(c == 0, pair, pair[::-1]).reshape(2 * own.shape[0], own.shape[1])

    packed = jnp.concatenate([
        small["norm_mix_g"], small["b_gate"].reshape(2, D_MODEL), small["norm_ffn_g"], small["norm_final_g"],
        jnp.pad(small["b_fgt"], ((0, 0), (0, D_MODEL - F_PAD))), jnp.zeros((2, D_MODEL), F32)], axis=0)
    summed = _sum_small(packed)
    for n in SMALL:
        lo, hi = SMALL_ROWS[n]
        grads[n] = summed[lo:hi].reshape(1, -1)[:, :weights[n].size]

    loss = lax.psum(loss_part[0, 0], ("x", "y", "c"))

    out_g, out_d, out_m, out_v = {}, {}, {}, {}
    for n in ORDER:
        shape = weights[n].shape
        two_d = shape[1:] if len(shape) == 3 else (1, weights[n].size)
        g2 = grads[n].reshape(two_d)
        d2, m2, v2 = _adamw(weights[n].reshape(two_d), g2, m_in[n].reshape(two_d), v_in[n].reshape(two_d),
                            name=f"adamw_{n}")
        out_g[n], out_d[n], out_m[n], out_v[n] = (g2.reshape(shape), d2.reshape(shape), m2.reshape(shape),
                                                  v2.reshape(shape))
    return (loss, grad_x[None], *[out_g[n] for n in ORDER], *[out_d[n] for n in ORDER],
            *[out_m[n] for n in ORDER], *[out_v[n] for n in ORDER])
```

```python
import numpy as np
import jax
import jax.numpy as jnp
from jax import lax
from jax.experimental import pallas as pl
from jax.experimental.pallas import tpu as pltpu

F32 = jnp.float32
_CD = jnp.bfloat16

D_MODEL = 1024
HEAD_DIM = 64
DIL_PAIRS = ((128, 1), (512, 4), (2048, 16))
N_DIL_GROUPS = 3
DIL_HEADS = 4
DIL_W = 128
DIL_OUT = DIL_HEADS * HEAD_DIM
DIL_WIDTH = N_DIL_GROUPS * DIL_OUT
N_FOX_HEADS = 8
FOX_WIDTH = N_FOX_HEADS * HEAD_DIM
D_FF = 2816
QKV_COLS = 3 * DIL_WIDTH + 3 * FOX_WIDTH
F_PAD = 128
RMS_EPS = 1e-6
NEG_INF = -1e30
ATTN_SCALE = HEAD_DIM ** -0.5
ADAM_LR, ADAM_B1, ADAM_B2, ADAM_EPS, ADAM_WD, ADAM_STEP = 0.001, 0.9, 0.999, 1e-08, 0.01, 10

VMEM_LIMIT = 48 * 1024 * 1024
MESH = pl.DeviceIdType.MESH
HBM_SPEC = pl.BlockSpec(memory_space=pltpu.HBM)


def _pcall(body, after=None, **kw):
    if after is None:
        return pl.pallas_call(body, **kw)
    n_in = len(kw["in_specs"])
    kw["in_specs"] = list(kw["in_specs"]) + [pl.BlockSpec(memory_space=pl.ANY)]

    def tied(*refs):
        return body(*refs[:n_in], *refs[n_in + 1:])

    call = pl.pallas_call(tied, **kw)
    return lambda *args: call(*args, after)


def _params(*sem):
    return pltpu.CompilerParams(dimension_semantics=sem, vmem_limit_bytes=VMEM_LIMIT)


def _pick(dim, pref):
    t = (min(pref, dim) // 128) * 128
    while t >= 128:
        if dim % t == 0:
            return t
        t -= 128
    return dim


def _mm(a, b, *, name, ta=False, tb=False, out_dtype=F32, add=None, tm=1024, tn=512, tk=2048, after=None,
        b_blocks=False, out_blocks=None):
    if ta:
        K, M = a.shape
    else:
        M, K = a.shape
    b_rows, b_cols = (b.shape[1], b.shape[0] * b.shape[2]) if b_blocks else b.shape
    if tb:
        N, K2 = b_rows, b_cols
    else:
        K2, N = b_rows, b_cols
    assert K == K2, (a.shape, b.shape)
    shard = b.shape[2] if b_blocks else None
    tm = _pick(M, tm)
    tn = _pick(shard if (b_blocks and not tb) else (out_blocks or N), tn)
    tk = _pick(shard if (b_blocks and tb) else K, tk)
    nk = K // tk
    dn = (((0 if ta else 1,), (1 if tb else 0,)), ((), ()))
    has_add = add is not None
    assert not (has_add and out_blocks)

    def body(*refs):
        a_ref, b_ref = refs[0], refs[1]
        add_ref = refs[2] if has_add else None
        o_ref = refs[3] if has_add else refs[2]
        bv = b_ref[0] if b_blocks else b_ref[...]
        p = lax.dot_general(a_ref[...].astype(_CD), bv.astype(_CD), dn, preferred_element_type=F32)

        def finish(r):
            if has_add:
                r = r + add_ref[...]
            if out_blocks:
                o_ref[0] = r.astype(out_dtype)
            else:
                o_ref[...] = r.astype(out_dtype)

        if nk == 1:
            finish(p)
        else:
            acc_ref = refs[-1]
            k = pl.program_id(2)

            @pl.when(k == 0)
            def _():
                acc_ref[...] = p

            @pl.when(k > 0)
            def _():
                acc_ref[...] += p

            @pl.when(k == nk - 1)
            def _():
                finish(acc_ref[...])

    a_spec = pl.BlockSpec((tk, tm), lambda i, j, k: (k, i)) if ta else pl.BlockSpec((tm, tk), lambda i, j, k: (i, k))
    if b_blocks and tb:
        per = shard // tk
        b_spec = pl.BlockSpec((1, tn, tk), lambda i, j, k: (k // per, j, k % per))
    elif b_blocks:
        per = shard // tn
        b_spec = pl.BlockSpec((1, tk, tn), lambda i, j, k: (j // per, k, j % per))
    else:
        b_spec = pl.BlockSpec((tn, tk), lambda i, j, k: (j, k)) if tb else pl.BlockSpec((tk, tn), lambda i, j, k: (k, j))
    if out_blocks:
        oper = out_blocks // tn
        o_spec = pl.BlockSpec((1, tm, tn), lambda i, j, k: (j // oper, i, j % oper))
        out_shape = jax.ShapeDtypeStruct((N // out_blocks, M, out_blocks), out_dtype)
    else:
        o_spec = pl.BlockSpec((tm, tn), lambda i, j, k: (i, j))
        out_shape = jax.ShapeDtypeStruct((M, N), out_dtype)
    in_specs = [a_spec, b_spec] + ([o_spec] if has_add else [])
    args = (a, b) + ((add,) if has_add else ())
    return _pcall(
        body, after, name=name, grid=(M // tm, N // tn, nk), in_specs=in_specs, out_specs=o_spec,
        out_shape=out_shape,
        scratch_shapes=[pltpu.VMEM((tm, tn), F32)] if nk > 1 else [],
        compiler_params=_params("parallel", "parallel", "arbitrary"),
    )(*args)


def _rms_fwd(x, g, *, name, tm=512, after=None):
    S, D = x.shape

    def body(x_ref, g_ref, h_ref):
        xv = x_ref[...]
        r = lax.rsqrt(jnp.mean(xv * xv, axis=-1, keepdims=True) + RMS_EPS)
        h_ref[...] = ((xv * r) * g_ref[...]).astype(h_ref.dtype)

    row = pl.BlockSpec((tm, D), lambda i: (i, 0))
    return _pcall(body, after, name=name, grid=(S // tm,), in_specs=[row, pl.BlockSpec((1, D), lambda i: (0, 0))],
                  out_specs=row, out_shape=jax.ShapeDtypeStruct((S, D), _CD), compiler_params=_params("parallel"))(x, g)


def _rms_bwd(x, g, dh, dres, *, name, tm=512, after=None):
    S, D = x.shape

    def body(x_ref, g_ref, dh_ref, dres_ref, dx_ref, dg_ref):
        xv = x_ref[...]
        r = lax.rsqrt(jnp.mean(xv * xv, axis=-1, keepdims=True) + RMS_EPS)
        xh = xv * r
        dhv = dh_ref[...]
        dxh = dhv * g_ref[...]
        dx_ref[...] = dres_ref[...] + r * (dxh - xh * jnp.mean(dxh * xh, axis=-1, keepdims=True))
        part = jnp.sum(dhv * xh, axis=0, keepdims=True)

        @pl.when(pl.program_id(0) == 0)
        def _():
            dg_ref[...] = part

        @pl.when(pl.program_id(0) > 0)
        def _():
            dg_ref[...] += part

    row = pl.BlockSpec((tm, D), lambda i: (i, 0))
    vec = pl.BlockSpec((1, D), lambda i: (0, 0))
    return _pcall(body, after, name=name, grid=(S // tm,), in_specs=[row, vec, row, row], out_specs=[row, vec],
                  out_shape=[jax.ShapeDtypeStruct((S, D), F32), jax.ShapeDtypeStruct((1, D), F32)],
                  compiler_params=_params("arbitrary"))(x, g, dh, dres)


def _loss_head(x, g, tgt, *, name, tm=512):
    S, D = x.shape

    def body(x_ref, g_ref, t_ref, loss_ref, dx_ref, dg_ref):
        xv = x_ref[...]
        gv = g_ref[...]
        r = lax.rsqrt(jnp.mean(xv * xv, axis=-1, keepdims=True) + RMS_EPS)
        xh = xv * r
        err = xh * gv - t_ref[...]
        lpart = 0.5 * jnp.sum(jnp.mean(err * err, axis=-1, keepdims=True), axis=0, keepdims=True)
        dy = err * (1.0 / D)
        dxh = dy * gv
        dx_ref[...] = r * (dxh - xh * jnp.mean(dxh * xh, axis=-1, keepdims=True))
        gpart = jnp.sum(dy * xh, axis=0, keepdims=True)

        @pl.when(pl.program_id(0) == 0)
        def _():
            loss_ref[...] = lpart
            dg_ref[...] = gpart

        @pl.when(pl.program_id(0) > 0)
        def _():
            loss_ref[...] += lpart
            dg_ref[...] += gpart

    row = pl.BlockSpec((tm, D), lambda i: (i, 0))
    vec = pl.BlockSpec((1, D), lambda i: (0, 0))
    one = pl.BlockSpec((1, 1), lambda i: (0, 0))
    return _pcall(body, name=name, grid=(S // tm,), in_specs=[row, vec, row], out_specs=[one, row, vec],
                  out_shape=[jax.ShapeDtypeStruct((1, 1), F32), jax.ShapeDtypeStruct((S, D), F32),
                             jax.ShapeDtypeStruct((1, D), F32)],
                  compiler_params=_params("arbitrary"))(x, g, tgt)


def _sigmoid(z):
    return 1.0 / (1.0 + jnp.exp(-z))


def _gate_fwd(gl, bg, ya, yb, *, name, tm=512):
    S, D = ya.shape

    def body(za_ref, zb_ref, ba_ref, bb_ref, ya_ref, yb_ref, o_ref):
        ga = _sigmoid(za_ref[...] + ba_ref[...])
        gb = _sigmoid(zb_ref[...] + bb_ref[...])
        o_ref[...] = (ga * ya_ref[...] + gb * yb_ref[...]).astype(o_ref.dtype)

    lo = pl.BlockSpec((tm, D), lambda i: (i, 0))
    hi = pl.BlockSpec((tm, D), lambda i: (i, 1))
    vlo = pl.BlockSpec((1, D), lambda i: (0, 0))
    vhi = pl.BlockSpec((1, D), lambda i: (0, 1))
    return _pcall(body, name=name, grid=(S // tm,), in_specs=[lo, hi, vlo, vhi, lo, lo], out_specs=lo,
                  out_shape=jax.ShapeDtypeStruct((S, D), _CD), compiler_params=_params("parallel"))(gl, gl, bg, bg, ya, yb)


def _gate_bwd(dm, gl, bg, ya, yb, *, name, tm=256):
    S, D = ya.shape

    def body(dm_ref, za_ref, zb_ref, ba_ref, bb_ref, ya_ref, yb_ref, dya_ref, dyb_ref, dgl_ref, dbg_ref):
        dmv = dm_ref[...]
        ga = _sigmoid(za_ref[...] + ba_ref[...])
        gb = _sigmoid(zb_ref[...] + bb_ref[...])
        dya_ref[...] = (dmv * ga).astype(dya_ref.dtype)
        dyb_ref[...] = (dmv * gb).astype(dyb_ref.dtype)
        dza = dmv * ya_ref[...] * ga * (1.0 - ga)
        dzb = dmv * yb_ref[...] * gb * (1.0 - gb)
        dgl_ref[:, :D] = dza.astype(dgl_ref.dtype)
        dgl_ref[:, D:] = dzb.astype(dgl_ref.dtype)
        pa = jnp.sum(dza, axis=0, keepdims=True)
        pb = jnp.sum(dzb, axis=0, keepdims=True)

        @pl.when(pl.program_id(0) == 0)
        def _():
            dbg_ref[:, :D] = pa
            dbg_ref[:, D:] = pb

        @pl.when(pl.program_id(0) > 0)
        def _():
            dbg_ref[:, :D] += pa
            dbg_ref[:, D:] += pb

    lo = pl.BlockSpec((tm, D), lambda i: (i, 0))
    hi = pl.BlockSpec((tm, D), lambda i: (i, 1))
    vlo = pl.BlockSpec((1, D), lambda i: (0, 0))
    vhi = pl.BlockSpec((1, D), lambda i: (0, 1))
    wide = pl.BlockSpec((tm, 2 * D), lambda i: (i, 0))
    vwide = pl.BlockSpec((1, 2 * D), lambda i: (0, 0))
    return _pcall(body, name=name, grid=(S // tm,), in_specs=[lo, lo, hi, vlo, vhi, lo, lo],
                  out_specs=[lo, lo, wide, vwide],
                  out_shape=[jax.ShapeDtypeStruct((S, D), _CD), jax.ShapeDtypeStruct((S, D), _CD),
                             jax.ShapeDtypeStruct((S, 2 * D), _CD), jax.ShapeDtypeStruct((1, 2 * D), F32)],
                  compiler_params=_params("arbitrary"))(dm, gl, gl, bg, bg, ya, yb)


def _swiglu_fwd(gu, *, name, tm=256):
    S, F2 = gu.shape
    F = F2 // 2

    def body(g_ref, u_ref, o_ref):
        gv = g_ref[...]
        o_ref[...] = (gv * _sigmoid(gv) * u_ref[...]).astype(o_ref.dtype)

    lo = pl.BlockSpec((tm, F), lambda i: (i, 0))
    hi = pl.BlockSpec((tm, F), lambda i: (i, 1))
    return _pcall(body, name=name, grid=(S // tm,), in_specs=[lo, hi], out_specs=lo,
                  out_shape=jax.ShapeDtypeStruct((S, F), _CD), compiler_params=_params("parallel"))(gu, gu)


def _swiglu_bwd(dact, gu, *, name, tm=256):
    S, F2 = gu.shape
    F = F2 // 2

    def body(d_ref, g_ref, u_ref, o_ref):
        dv = d_ref[...]
        gv = g_ref[...]
        sg = _sigmoid(gv)
        o_ref[:, :F] = (dv * u_ref[...] * (sg * (1.0 + gv * (1.0 - sg)))).astype(o_ref.dtype)
        o_ref[:, F:] = (dv * (gv * sg)).astype(o_ref.dtype)

    lo = pl.BlockSpec((tm, F), lambda i: (i, 0))
    hi = pl.BlockSpec((tm, F), lambda i: (i, 1))
    return _pcall(body, name=name, grid=(S // tm,), in_specs=[lo, lo, hi],
                  out_specs=pl.BlockSpec((tm, F2), lambda i: (i, 0)),
                  out_shape=jax.ShapeDtypeStruct((S, F2), _CD), compiler_params=_params("parallel"))(dact, gu, gu)


def _split3(x):
    hi = x.astype(jnp.bfloat16)
    r1 = x - hi.astype(F32)
    mid = r1.astype(jnp.bfloat16)
    lo = (r1 - mid.astype(F32)).astype(jnp.bfloat16)
    return hi, mid, lo


def _ones_dot_left(ones, x):
    return sum(jnp.dot(ones, p, preferred_element_type=F32) for p in _split3(x))


def _ones_dot_right(x, ones):
    return sum(jnp.dot(p, ones, preferred_element_type=F32) for p in _split3(x))


def _head_sum(x):
    n = x.shape[1]
    r = lax.broadcasted_iota(jnp.int32, (n, n), 0) // HEAD_DIM
    c = lax.broadcasted_iota(jnp.int32, (n, n), 1) // HEAD_DIM
    return _ones_dot_right(x, (r == c).astype(jnp.bfloat16))


def _log_sigmoid(z):
    e = jnp.exp(-jnp.abs(z))
    t = 1.0 + e
    log1p_e = jnp.where(t == 1.0, e, jnp.log(t) * (e / jnp.where(t == 1.0, 1.0, t - 1.0)))
    return jnp.minimum(z, 0.0) - log1p_e


def _fox_cumsum(zf, bf, *, name):
    S, W = zf.shape
    nb = S // 128

    def body(z_ref, b_ref, c_ref):
        tri = (lax.broadcasted_iota(jnp.int32, (128, 128), 0) >= lax.broadcasted_iota(jnp.int32, (128, 128), 1))
        tri = tri.astype(jnp.bfloat16)

        def step(i, carry):
            rows = pl.ds(pl.multiple_of(i * 128, 128), 128)
            lf = _log_sigmoid(z_ref[rows, :] + b_ref[...])
            cb = _ones_dot_left(tri, lf) + carry
            c_ref[rows, :] = cb
            return cb[127:128, :]

        lax.fori_loop(0, nb, step, jnp.zeros((1, W), F32))

    return _pcall(body, name=name, out_shape=jax.ShapeDtypeStruct((S, W), F32),
                  compiler_params=pltpu.CompilerParams(vmem_limit_bytes=VMEM_LIMIT))(zf, bf)


def _fox_cumsum_bwd(dc, zf, bf, *, name):
    S, W = zf.shape
    nb = S // 128

    def body(dc_ref, z_ref, b_ref, dz_ref, db_ref):
        tri = (lax.broadcasted_iota(jnp.int32, (128, 128), 0) <= lax.broadcasted_iota(jnp.int32, (128, 128), 1))
        tri = tri.astype(jnp.bfloat16)

        def step(k, carry):
            tail, acc = carry
            i = nb - 1 - k
            rows = pl.ds(pl.multiple_of(i * 128, 128), 128)
            dlf = _ones_dot_left(tri, dc_ref[rows, :]) + tail
            dz = dlf * _sigmoid(-(z_ref[rows, :] + b_ref[...]))
            dz_ref[rows, :] = dz
            return dlf[0:1, :], acc + jnp.sum(dz, axis=0, keepdims=True)

        _, acc = lax.fori_loop(0, nb, step, (jnp.zeros((1, W), F32), jnp.zeros((1, W), F32)))
        db_ref[...] = acc

    return _pcall(body, name=name,
                  out_shape=[jax.ShapeDtypeStruct((S, W), F32), jax.ShapeDtypeStruct((1, W), F32)],
                  compiler_params=pltpu.CompilerParams(vmem_limit_bytes=VMEM_LIMIT))(dc, zf, bf)


def _dil_slopes(group):
    h = np.arange(1, N_DIL_GROUPS * DIL_HEADS + 1, dtype=np.float32)
    s = (np.float32(2.0) ** (np.float32(-8.0) * h / np.float32(N_DIL_GROUPS * DIL_HEADS))).astype(np.float32)
    return [float(v) for v in s.reshape(N_DIL_GROUPS, DIL_HEADS)[group]]


def _dil_tiles(i, n, blocks_per_seq):
    qi = lax.broadcasted_iota(jnp.int32, (DIL_W, DIL_W), 0)
    kj = lax.broadcasted_iota(jnp.int32, (DIL_W, DIL_W), 1)
    first = ((4 * n + i) % blocks_per_seq) == 0
    valid_prev = jnp.logical_and(kj >= qi, jnp.logical_not(first))
    valid_cur = kj <= qi
    rel_prev = (qi - kj + DIL_W).astype(F32)
    rel_cur = (qi - kj).astype(F32)
    return valid_prev, valid_cur, rel_prev, rel_cur


CHUNK = 4 * DIL_W


def _dil_fwd(q, k, v, group, *, name):
    S = q.shape[0]
    dilation = DIL_PAIRS[group][1]
    bps = (S // dilation) // DIL_W
    slopes = _dil_slopes(group)
    nt = (((1,), (1,)), ((), ()))

    def body(q_ref, k_ref, v_ref, kp_ref, vp_ref, o_ref, l_ref):
        n = pl.program_id(0)
        for i in range(4):
            valid_prev, valid_cur, rel_prev, rel_cur = _dil_tiles(i, n, bps)
            rows = slice(i * DIL_W, (i + 1) * DIL_W)
            prow = slice((i - 1) * DIL_W, i * DIL_W)
            for h in range(DIL_HEADS):
                cols = slice(h * HEAD_DIM, (h + 1) * HEAD_DIM)
                qh = q_ref[rows, cols]
                kc, vc = k_ref[rows, cols], v_ref[rows, cols]
                kp = kp_ref[:, cols] if i == 0 else k_ref[prow, cols]
                vp = vp_ref[:, cols] if i == 0 else v_ref[prow, cols]
                sl = slopes[h] * dilation
                sp = lax.dot_general(qh, kp, nt, preferred_element_type=F32) * ATTN_SCALE - sl * rel_prev
                sc = lax.dot_general(qh, kc, nt, preferred_element_type=F32) * ATTN_SCALE - sl * rel_cur
                sp = jnp.where(valid_prev, sp, NEG_INF)
                sc = jnp.where(valid_cur, sc, NEG_INF)
                m = jnp.maximum(jnp.max(sp, axis=-1, keepdims=True), jnp.max(sc, axis=-1, keepdims=True))
                pp, pc = jnp.exp(sp - m), jnp.exp(sc - m)
                den = jnp.sum(pp, axis=-1, keepdims=True) + jnp.sum(pc, axis=-1, keepdims=True)
                acc = (jnp.dot(pp.astype(_CD), vp, preferred_element_type=F32)
                       + jnp.dot(pc.astype(_CD), vc, preferred_element_type=F32))
                o_ref[rows, cols] = acc / den
                l_ref[rows, cols] = jnp.broadcast_to(m + jnp.log(den), (DIL_W, HEAD_DIM))

    cur = pl.BlockSpec((CHUNK, DIL_OUT), lambda n: (n, 0))
    prev = pl.BlockSpec((DIL_W, DIL_OUT), lambda n: (jnp.maximum(4 * n - 1, 0), 0))
    return _pcall(body, name=name, grid=(S // CHUNK,), in_specs=[cur, cur, cur, prev, prev], out_specs=[cur, cur],
                  out_shape=[jax.ShapeDtypeStruct((S, DIL_OUT), F32), jax.ShapeDtypeStruct((S, DIL_OUT), F32)],
                  compiler_params=_params("parallel"))(q, k, v, k, v)


def _dil_bwd(q, k, v, o, lse, do, dlse, group, *, name):
    S = q.shape[0]
    dilation = DIL_PAIRS[group][1]
    bps = (S // dilation) // DIL_W
    slopes = _dil_slopes(group)
    nchunk = S // CHUNK
    nt = (((1,), (1,)), ((), ()))
    tn = (((0,), (0,)), ((), ()))

    def body(q_ref, k_ref, v_ref, kp_ref, vp_ref, o_ref, l_ref, do_ref, dl_ref, dq_ref, dk_ref, dv_ref,
             dk_s, dv_s):
        step = pl.program_id(0)
        n = nchunk - 1 - step

        @pl.when(step == 0)
        def _():
            dk_s[CHUNK:, :] = jnp.zeros((DIL_W, DIL_OUT), F32)
            dv_s[CHUNK:, :] = jnp.zeros((DIL_W, DIL_OUT), F32)

        dk_s[:CHUNK, :] = jnp.zeros((CHUNK, DIL_OUT), F32)
        dv_s[:CHUNK, :] = jnp.zeros((CHUNK, DIL_OUT), F32)
        for i in range(4):
            valid_prev, valid_cur, rel_prev, rel_cur = _dil_tiles(i, n, bps)
            rows = slice(i * DIL_W, (i + 1) * DIL_W)
            prow = slice((i - 1) * DIL_W, i * DIL_W)
            s_prev = slice(i * DIL_W, (i + 1) * DIL_W)
            s_cur = slice((i + 1) * DIL_W, (i + 2) * DIL_W)
            for h in range(DIL_HEADS):
                cols = slice(h * HEAD_DIM, (h + 1) * HEAD_DIM)
                qh = q_ref[rows, cols]
                kc, vc = k_ref[rows, cols], v_ref[rows, cols]
                kp = kp_ref[:, cols] if i == 0 else k_ref[prow, cols]
                vp = vp_ref[:, cols] if i == 0 else v_ref[prow, cols]
                sl = slopes[h] * dilation
                lh = l_ref[rows, h * HEAD_DIM:h * HEAD_DIM + 1]
                sp = lax.dot_general(qh, kp, nt, preferred_element_type=F32) * ATTN_SCALE - sl * rel_prev
                sc = lax.dot_general(qh, kc, nt, preferred_element_type=F32) * ATTN_SCALE - sl * rel_cur
                pp = jnp.exp(jnp.where(valid_prev, sp, NEG_INF) - lh)
                pc = jnp.exp(jnp.where(valid_cur, sc, NEG_INF) - lh)
                doh = do_ref[rows, cols]
                dsum = jnp.sum(doh * o_ref[rows, cols], axis=-1, keepdims=True)
                shift = dl_ref[rows, h * HEAD_DIM:h * HEAD_DIM + 1] - dsum
                dob = doh.astype(_CD)
                dsp = pp * (lax.dot_general(dob, vp, nt, preferred_element_type=F32) + shift)
                dsc = pc * (lax.dot_general(dob, vc, nt, preferred_element_type=F32) + shift)
                dspb = (dsp * ATTN_SCALE).astype(_CD)
                dscb = (dsc * ATTN_SCALE).astype(_CD)
                dq_ref[rows, cols] = (jnp.dot(dspb, kp, preferred_element_type=F32)
                                      + jnp.dot(dscb, kc, preferred_element_type=F32)).astype(dq_ref.dtype)
                dk_s[s_prev, cols] += lax.dot_general(dspb, qh, tn, preferred_element_type=F32)
                dk_s[s_cur, cols] += lax.dot_general(dscb, qh, tn, preferred_element_type=F32)
                dv_s[s_prev, cols] += lax.dot_general(pp.astype(_CD), dob, tn, preferred_element_type=F32)
                dv_s[s_cur, cols] += lax.dot_general(pc.astype(_CD), dob, tn, preferred_element_type=F32)
        dk_ref[...] = dk_s[DIL_W:, :].astype(dk_ref.dtype)
        dv_ref[...] = dv_s[DIL_W:, :].astype(dv_ref.dtype)
        dk_s[CHUNK:, :] = dk_s[:DIL_W, :]
        dv_s[CHUNK:, :] = dv_s[:DIL_W, :]

    cur = pl.BlockSpec((CHUNK, DIL_OUT), lambda s: (nchunk - 1 - s, 0))
    prev = pl.BlockSpec((DIL_W, DIL_OUT), lambda s: (jnp.maximum(4 * (nchunk - 1 - s) - 1, 0), 0))
    shp = jax.ShapeDtypeStruct((S, DIL_OUT), _CD)
    return _pcall(body, name=name, grid=(nchunk,), in_specs=[cur, cur, cur, prev, prev, cur, cur, cur, cur],
                  out_specs=[cur, cur, cur], out_shape=[shp, shp, shp],
                  scratch_shapes=[pltpu.VMEM((CHUNK + DIL_W, DIL_OUT), F32), pltpu.VMEM((CHUNK + DIL_W, DIL_OUT), F32)],
                  compiler_params=_params("arbitrary"))(q, k, v, k, v, o, lse, do, dlse)


def _dil_mix_fwd(os_, ls_, *, name, tm=512):
    S, W = os_[0].shape

    def body(o0, o1, o2, l0, l1, l2, out_ref):
        ls = [l0[...], l1[...], l2[...]]
        m = jnp.maximum(jnp.maximum(ls[0], ls[1]), ls[2])
        es = [jnp.exp(l - m) for l in ls]
        den = es[0] + es[1] + es[2]
        out_ref[...] = ((es[0] * o0[...] + es[1] * o1[...] + es[2] * o2[...]) / den).astype(out_ref.dtype)

    row = pl.BlockSpec((tm, W), lambda i: (i, 0))
    return _pcall(body, name=name, grid=(S // tm,), in_specs=[row] * 6, out_specs=row,
                  out_shape=jax.ShapeDtypeStruct((S, W), _CD), compiler_params=_params("parallel"))(*os_, *ls_)


def _dil_mix_bwd(doa, os_, ls_, *, name, tm=512, after=None):
    S, W = doa.shape

    def body(d_ref, o0, o1, o2, l0, l1, l2, do0, do1, do2, dl0, dl1, dl2):
        dv = d_ref[...]
        ls = [l0[...], l1[...], l2[...]]
        m = jnp.maximum(jnp.maximum(ls[0], ls[1]), ls[2])
        es = [jnp.exp(l - m) for l in ls]
        den = es[0] + es[1] + es[2]
        al = [e / den for e in es]
        da = [_head_sum(dv * o[...]) for o in (o0, o1, o2)]
        mean = al[0] * da[0] + al[1] * da[1] + al[2] * da[2]
        for a, d_, do_ref, dl_ref in zip(al, da, (do0, do1, do2), (dl0, dl1, dl2)):
            do_ref[...] = a * dv
            dl_ref[...] = a * (d_ - mean)

    row = pl.BlockSpec((tm, W), lambda i: (i, 0))
    shp = jax.ShapeDtypeStruct((S, W), F32)
    return _pcall(body, after, name=name, grid=(S // tm,), in_specs=[row] * 7, out_specs=[row] * 6, out_shape=[shp] * 6,
                  compiler_params=_params("parallel"))(doa, *os_, *ls_)


FOX_T = 512


PACK = 2 * HEAD_DIM
HEAD_PAIRS = N_FOX_HEADS // 2
Q_BLOCK0 = (3 * DIL_WIDTH) // PACK
K_BLOCK0 = (3 * DIL_WIDTH + FOX_WIDTH) // PACK
V_BLOCK0 = (3 * DIL_WIDTH + 2 * FOX_WIDTH) // PACK


def _pieces(x):
    hi = x.astype(jnp.bfloat16).astype(F32)
    r = x - hi
    mid = r.astype(jnp.bfloat16).astype(F32)
    lo = (r - mid).astype(jnp.bfloat16).astype(F32)
    return [hi, mid, lo]


def _extras(first, second, rows):
    lane = lax.broadcasted_iota(jnp.int32, (rows, HEAD_DIM), 1)
    out = jnp.zeros((rows, HEAD_DIM), F32)
    for idx, val in enumerate(list(first) + list(second)):
        out = jnp.where(lane == idx, val, out)
    return out


def _head_column(c, h):
    lane = lax.broadcasted_iota(jnp.int32, c.shape, 1)
    return jnp.sum(jnp.where(lane == h, c, 0.0), axis=1, keepdims=True)


ONES3 = [1.0, 1.0, 1.0]
ZEROS3 = [0.0, 0.0, 0.0]


def _fox_pack_fwd(qkv, c, *, name, tm=512):
    S = qkv.shape[0]

    def body(q_ref, k_ref, v_ref, c_ref, qo_ref, ko_ref, vo_ref):
        hp = pl.program_id(1)
        cv = c_ref[...]
        for hh in range(2):
            ch = _pieces(_head_column(cv, 2 * hp + hh))
            src = slice(hh * HEAD_DIM, (hh + 1) * HEAD_DIM)
            lo = slice(hh * PACK, hh * PACK + HEAD_DIM)
            hi = slice(hh * PACK + HEAD_DIM, (hh + 1) * PACK)
            qo_ref[:, lo] = (q_ref[:, src].astype(F32) * ATTN_SCALE).astype(qo_ref.dtype)
            qo_ref[:, hi] = _extras(ch, ONES3, tm).astype(qo_ref.dtype)
            ko_ref[:, lo] = k_ref[:, src]
            ko_ref[:, hi] = _extras(ONES3, [-p for p in ch], tm).astype(ko_ref.dtype)
            vo_ref[:, lo] = v_ref[:, src]
            vo_ref[:, hi] = _extras(ONES3, ZEROS3, tm).astype(vo_ref.dtype)

    def src(block0):
        return pl.BlockSpec((tm, PACK), lambda i, hp: (i, block0 + hp))

    out = pl.BlockSpec((tm, 2 * PACK), lambda i, hp: (i, hp))
    shp = jax.ShapeDtypeStruct((S, N_FOX_HEADS * PACK), _CD)
    return _pcall(body, name=name, grid=(S // tm, HEAD_PAIRS),
                  in_specs=[src(Q_BLOCK0), src(K_BLOCK0), src(V_BLOCK0), pl.BlockSpec((tm, PACK), lambda i, hp: (i, 0))],
                  out_specs=[out, out, out], out_shape=[shp, shp, shp],
                  compiler_params=_params("parallel", "parallel"))(qkv, qkv, qkv, c)


def _fox_fwd(qp, kp, vp, *, name):
    S = qp.shape[0]
    nt = S // FOX_T
    nt_dims = (((1,), (1,)), ((), ()))
    tn_dims = (((0,), (0,)), ((), ()))

    def body(i_tab, j_tab, q_ref, k_ref, v_ref, o_ref, l_ref, m_s, acc_s):
        t = pl.program_id(1)
        i, j = i_tab[t], j_tab[t]

        @pl.when(j == 0)
        def _():
            m_s[...] = jnp.full((2, 1, FOX_T), NEG_INF, F32)
            acc_s[...] = jnp.zeros((2, PACK, FOX_T), F32)

        def tile(diagonal):
            for hh in range(2):
                cols = slice(hh * PACK, (hh + 1) * PACK)
                st = lax.dot_general(k_ref[:, cols], q_ref[:, cols], nt_dims, preferred_element_type=F32)
                if diagonal:
                    key = lax.broadcasted_iota(jnp.int32, (FOX_T, FOX_T), 0)
                    qry = lax.broadcasted_iota(jnp.int32, (FOX_T, FOX_T), 1)
                    st = jnp.where(key <= qry, st, NEG_INF)
                m_old = m_s[hh]
                m_new = jnp.maximum(m_old, jnp.max(st, axis=0, keepdims=True))
                pt = jnp.exp(st - m_new)
                acc_s[hh] = jnp.exp(m_old - m_new) * acc_s[hh] + lax.dot_general(
                    v_ref[:, cols], pt.astype(_CD), tn_dims, preferred_element_type=F32)
                m_s[hh] = m_new

        @pl.when(j < i)
        def _():
            tile(False)

        @pl.when(j == i)
        def _():
            tile(True)
            for hh in range(2):
                acc = acc_s[hh]
                den = acc[HEAD_DIM:HEAD_DIM + 1, :]
                cols = slice(hh * HEAD_DIM, (hh + 1) * HEAD_DIM)
                o_ref[:, cols] = (acc[:HEAD_DIM, :] / den).T
                l_ref[:, cols] = jnp.broadcast_to(m_s[hh] + jnp.log(den), (HEAD_DIM, FOX_T)).T

    pairs = [(i, j) for i in range(nt) for j in range(i + 1)]
    i_tab = jnp.asarray([p[0] for p in pairs], jnp.int32)
    j_tab = jnp.asarray([p[1] for p in pairs], jnp.int32)
    qs = pl.BlockSpec((FOX_T, 2 * PACK), lambda hp, t, it, jt: (it[t], hp))
    ks = pl.BlockSpec((FOX_T, 2 * PACK), lambda hp, t, it, jt: (jt[t], hp))
    os_ = pl.BlockSpec((FOX_T, PACK), lambda hp, t, it, jt: (it[t], hp))
    shp = jax.ShapeDtypeStruct((S, FOX_WIDTH), F32)
    grid_spec = pltpu.PrefetchScalarGridSpec(
        num_scalar_prefetch=2, grid=(HEAD_PAIRS, len(pairs)), in_specs=[qs, ks, ks], out_specs=[os_, os_],
        scratch_shapes=[pltpu.VMEM((2, 1, FOX_T), F32), pltpu.VMEM((2, PACK, FOX_T), F32)])
    return _pcall(body, name=name, grid_spec=grid_spec, out_shape=[shp, shp],
                  compiler_params=_params("parallel", "arbitrary"))(i_tab, j_tab, qp, kp, vp)


def _fox_pack_bwd(qkv, c, o, lse, do, *, name, tm=512, after=None):
    S = qkv.shape[0]

    def body(q_ref, c_ref, o_ref, l_ref, do_ref, qo_ref, do_out_ref):
        hp = pl.program_id(1)
        cv = c_ref[...]
        for hh in range(2):
            src = slice(hh * HEAD_DIM, (hh + 1) * HEAD_DIM)
            lo = slice(hh * PACK, hh * PACK + HEAD_DIM)
            hi = slice(hh * PACK + HEAD_DIM, (hh + 1) * PACK)
            shift = _head_column(cv, 2 * hp + hh) - l_ref[:, hh * HEAD_DIM:hh * HEAD_DIM + 1]
            dov = do_ref[:, src]
            dsum = jnp.sum(dov * o_ref[:, src], axis=-1, keepdims=True)
            qo_ref[:, lo] = (q_ref[:, src].astype(F32) * ATTN_SCALE).astype(qo_ref.dtype)
            qo_ref[:, hi] = _extras(_pieces(shift), ONES3, tm).astype(qo_ref.dtype)
            do_out_ref[:, lo] = dov.astype(do_out_ref.dtype)
            do_out_ref[:, hi] = _extras(_pieces(-dsum), ZEROS3, tm).astype(do_out_ref.dtype)

    pair = pl.BlockSpec((tm, PACK), lambda i, hp: (i, hp))
    out = pl.BlockSpec((tm, 2 * PACK), lambda i, hp: (i, hp))
    shp = jax.ShapeDtypeStruct((S, N_FOX_HEADS * PACK), _CD)
    return _pcall(body, after, name=name, grid=(S // tm, HEAD_PAIRS),
                  in_specs=[pl.BlockSpec((tm, PACK), lambda i, hp: (i, Q_BLOCK0 + hp)),
                            pl.BlockSpec((tm, PACK), lambda i, hp: (i, 0)), pair, pair, pair],
                  out_specs=[out, out], out_shape=[shp, shp],
                  compiler_params=_params("parallel", "parallel"))(qkv, c, o, lse, do)


def _fox_bwd(qp, kp, vp, dop, *, name):
    S = qp.shape[0]
    nt = S // FOX_T
    nt_dims = (((1,), (1,)), ((), ()))
    tn_dims = (((0,), (0,)), ((), ()))

    def body(i_tab, j_tab, q_ref, k_ref, v_ref, do_ref, dq_ref, dk_ref, dv_ref, dc_ref, dr_ref,
             dq_s, dk_s, dv_s, dc_s, dr_s):
        t = pl.program_id(1)
        i, j = i_tab[t], j_tab[t]

        @pl.when(t == 0)
        def _():
            dq_s[...] = jnp.zeros((S, 2 * PACK), F32)
            dr_s[...] = jnp.zeros((2, 1, S), F32)

        @pl.when(i == j)
        def _():
            dk_s[...] = jnp.zeros((FOX_T, 2 * PACK), F32)
            dv_s[...] = jnp.zeros((FOX_T, 2 * PACK), F32)
            dc_s[...] = jnp.zeros((2, FOX_T, 1), F32)

        def tile(diagonal):
            rows = pl.ds(pl.multiple_of(i * FOX_T, FOX_T), FOX_T)
            for hh in range(2):
                cols = slice(hh * PACK, (hh + 1) * PACK)
                qv, kv, vv, dov = q_ref[:, cols], k_ref[:, cols], v_ref[:, cols], do_ref[:, cols]
                pt = jnp.exp(lax.dot_general(kv, qv, nt_dims, preferred_element_type=F32))
                if diagonal:
                    key = lax.broadcasted_iota(jnp.int32, (FOX_T, FOX_T), 0)
                    qry = lax.broadcasted_iota(jnp.int32, (FOX_T, FOX_T), 1)
                    pt = jnp.where(key <= qry, pt, 0.0)
                dst = pt * lax.dot_general(vv, dov, nt_dims, preferred_element_type=F32)
                dsb = dst.astype(_CD)
                dc_s[hh] += jnp.sum(dst, axis=1, keepdims=True)
                dr_s[hh, :, rows] += jnp.sum(dst, axis=0, keepdims=True)
                dv_s[:, cols] += jnp.dot(pt.astype(_CD), dov, preferred_element_type=F32)
                dk_s[:, cols] += jnp.dot(dsb, qv, preferred_element_type=F32)
                dq_s[rows, cols] += lax.dot_general(dsb, kv, tn_dims, preferred_element_type=F32)

        @pl.when(i > j)
        def _():
            tile(False)

        @pl.when(i == j)
        def _():
            tile(True)

        @pl.when(i == nt - 1)
        def _():
            for hh in range(2):
                src = slice(hh * PACK, hh * PACK + HEAD_DIM)
                dst_cols = slice(hh * HEAD_DIM, (hh + 1) * HEAD_DIM)
                dk_ref[:, dst_cols] = dk_s[:, src].astype(dk_ref.dtype)
                dv_ref[:, dst_cols] = dv_s[:, src].astype(dv_ref.dtype)
                dc_ref[:, dst_cols] = jnp.broadcast_to(dc_s[hh], (FOX_T, HEAD_DIM))

        @pl.when(t == len(pairs) - 1)
        def _():
            for hh in range(2):
                dq_ref[:, hh * HEAD_DIM:(hh + 1) * HEAD_DIM] = (
                    dq_s[:, hh * PACK:hh * PACK + HEAD_DIM] * ATTN_SCALE).astype(dq_ref.dtype)
            dr_ref[...] = dr_s[...]

    pairs = [(i, j) for j in range(nt) for i in range(j, nt)]
    i_tab = jnp.asarray([p[0] for p in pairs], jnp.int32)
    j_tab = jnp.asarray([p[1] for p in pairs], jnp.int32)
    qs = pl.BlockSpec((FOX_T, 2 * PACK), lambda hp, t, it, jt: (it[t], hp))
    ks = pl.BlockSpec((FOX_T, 2 * PACK), lambda hp, t, it, jt: (jt[t], hp))
    whole = pl.BlockSpec((S, PACK), lambda hp, t, it, jt: (0, hp))
    cs = pl.BlockSpec((FOX_T, PACK), lambda hp, t, it, jt: (jt[t], hp))
    rs = pl.BlockSpec((2, 1, S), lambda hp, t, it, jt: (hp, 0, 0))
    shp = jax.ShapeDtypeStruct((S, FOX_WIDTH), _CD)
    grid_spec = pltpu.PrefetchScalarGridSpec(
        num_scalar_prefetch=2, grid=(HEAD_PAIRS, len(pairs)), in_specs=[qs, ks, ks, qs],
        out_specs=[whole, cs, cs, cs, rs],
        scratch_shapes=[pltpu.VMEM((S, 2 * PACK), F32), pltpu.VMEM((FOX_T, 2 * PACK), F32),
                        pltpu.VMEM((FOX_T, 2 * PACK), F32), pltpu.VMEM((2, FOX_T, 1), F32),
                        pltpu.VMEM((2, 1, S), F32)])
    return _pcall(body, name=name, grid_spec=grid_spec,
                  out_shape=[shp, shp, shp, jax.ShapeDtypeStruct((S, FOX_WIDTH), F32),
                             jax.ShapeDtypeStruct((N_FOX_HEADS, 1, S), F32)],
                  compiler_params=_params("parallel", "arbitrary"))(i_tab, j_tab, qp, kp, vp, dop)


def _dedilate(t, d):
    if d == 1:
        return t
    S, C = t.shape
    return t.reshape(S // d, d, C).transpose(1, 0, 2).reshape(S, C)


def _redilate(t, d):
    if d == 1:
        return t
    S, C = t.shape
    return t.reshape(d, S // d, C).transpose(1, 0, 2).reshape(S, C)


def _layer_step(x, tgt, w, p, late_weights=None, grad_sink=None, after=None):
    S = x.shape[0]
    h = _rms_fwd(x, p["norm_mix_g"], name="rms_mix")
    qkv = _mm(h, w["qkv"], name="proj_qkv", out_dtype=_CD, tn=768, after=after)
    zf = _mm(h, w["f"], name="proj_f")
    gl = _mm(h, w["g"], name="proj_gate", tn=1024)

    dil_q, dil_k, dil_v = [], [], []
    dil_o, dil_l = [], []
    for g, (_, d) in enumerate(DIL_PAIRS):
        qg = _dedilate(qkv[:, g * DIL_OUT:(g + 1) * DIL_OUT], d)
        kg = _dedilate(qkv[:, DIL_WIDTH + g * DIL_OUT:DIL_WIDTH + (g + 1) * DIL_OUT], d)
        vg = _dedilate(qkv[:, 2 * DIL_WIDTH + g * DIL_OUT:2 * DIL_WIDTH + (g + 1) * DIL_OUT], d)
        og, lg = _dil_fwd(qg, kg, vg, g, name=f"dil_fwd{g}")
        dil_q.append(qg), dil_k.append(kg), dil_v.append(vg)
        dil_o.append(_redilate(og, d)), dil_l.append(_redilate(lg, d))
    o_a = _dil_mix_fwd(dil_o, dil_l, name="dil_mix")

    c = _fox_cumsum(zf, p["b_fgt"], name="fox_cumsum")
    fqp, fkp, fvp = _fox_pack_fwd(qkv, c, name="fox_pack")
    o_b, flse = _fox_fwd(fqp, fkp, fvp, name="fox_fwd")

    if late_weights is not None:
        w = {**w, **late_weights(o_b)}
    y_a = _mm(o_a, w["dil_out"], name="y_a", tn=1024)
    y_b = _mm(o_b, w["fox_out"], name="y_b", tn=1024)
    merged = _gate_fwd(gl, p["b_gate"], y_a, y_b, name="gate_fwd")
    x1 = _mm(merged, w["out"], name="mix_out", add=x)

    h2 = _rms_fwd(x1, p["norm_ffn_g"], name="rms_ffn")
    gu = _mm(h2, w["ffn_in"], name="ffn_in", tn=1408, b_blocks=True)
    act = _swiglu_fwd(gu, name="swiglu")
    x2 = _mm(act, w["ffn_down"], name="ffn_down", add=x1, tk=2816)

    loss, dx2, dg_final = _loss_head(x2, p["norm_final_g"], tgt, name="loss_head")

    dact = _mm(dx2, w["ffn_down"], name="d_act", tb=True, tn=1408)
    gw_ffn_down = _mm(act, dx2, name="gw_ffn_down", ta=True, out_dtype=_CD, tm=1408)
    dgu = _swiglu_bwd(dact, gu, name="swiglu_bwd")
    dh2 = _mm(dgu, w["ffn_in"], name="d_h2", tb=True, tk=1408, b_blocks=True)
    gw_ffn_in = _mm(h2, dgu, name="gw_ffn_in", ta=True, out_dtype=_CD, tn=1408, out_blocks=1408)
    sink = grad_sink if grad_sink is not None else (lambda group, grads: None)
    tok = sink("ffn", dict(ffn_in=gw_ffn_in, ffn_down=gw_ffn_down))
    dx1, dg_ffn = _rms_bwd(x1, p["norm_ffn_g"], dh2, dx2, name="rms_ffn_bwd", after=tok)

    dmerged = _mm(dx1, w["out"], name="d_merged", tb=True)
    gw_out = _mm(merged, dx1, name="gw_out", ta=True, out_dtype=_CD)
    dy_a, dy_b, dgl, db_gate = _gate_bwd(dmerged, gl, p["b_gate"], y_a, y_b, name="gate_bwd")
    do_a = _mm(dy_a, w["dil_out"], name="d_o_a", tb=True)
    gw_dil_out = _mm(o_a, dy_a, name="gw_dil_out", ta=True, out_dtype=_CD, tn=1024)
    do_b = _mm(dy_b, w["fox_out"], name="d_o_b", tb=True)
    gw_fox_out = _mm(o_b, dy_b, name="gw_fox_out", ta=True, out_dtype=_CD, tn=1024)
    tok = sink("mix", dict(dil_out=gw_dil_out, fox_out=gw_fox_out, out=gw_out))

    bqp, bdop = _fox_pack_bwd(qkv, c, o_b, flse, do_b, name="fox_pack_bwd", after=tok)
    dqp, dkp, dvp, dck, dcq = _fox_bwd(bqp, fkp, fvp, bdop, name="fox_bwd")
    dc = dcq[:, 0, :].T - dck.reshape(S, N_FOX_HEADS, HEAD_DIM)[:, :, 0]
    dc = jnp.pad(dc, ((0, 0), (0, F_PAD - N_FOX_HEADS)))
    dzf, db_fgt = _fox_cumsum_bwd(dc, zf, p["b_fgt"], name="fox_cumsum_bwd")

    douts = _dil_mix_bwd(do_a, dil_o, dil_l, name="dil_mix_bwd", after=tok)
    dqs, dks, dvs = [], [], []
    for g, (_, d) in enumerate(DIL_PAIRS):
        dq, dk, dv = _dil_bwd(dil_q[g], dil_k[g], dil_v[g], _dedilate(dil_o[g], d), _dedilate(dil_l[g], d),
                              _dedilate(douts[g], d), _dedilate(douts[3 + g], d), g, name=f"dil_bwd{g}")
        dqs.append(_redilate(dq, d)), dks.append(_redilate(dk, d)), dvs.append(_redilate(dv, d))
    dqkv = jnp.concatenate(dqs + dks + dvs + [dqp, dkp, dvp], axis=1)

    gw_qkv = _mm(h, dqkv, name="gw_qkv", ta=True, out_dtype=_CD, tn=768)
    gw_g = _mm(h, dgl, name="gw_gate", ta=True, out_dtype=_CD)
    gw_f = _mm(h, dzf, name="gw_f", ta=True, out_dtype=_CD)
    tok = sink("in", dict(qkv=gw_qkv, f=gw_f, g=gw_g))
    dh = _mm(dqkv, w["qkv"], name="d_h_qkv", tb=True, tk=1920, after=tok)
    dh = _mm(dgl, w["g"], name="d_h_gate", tb=True, add=dh)
    dh = _mm(dzf, w["f"], name="d_h_f", tb=True, add=dh)
    dx, dg_mix = _rms_bwd(x, p["norm_mix_g"], dh, dx1, name="rms_mix_bwd")

    gw = dict(qkv=gw_qkv, f=gw_f, g=gw_g, dil_out=gw_dil_out, fox_out=gw_fox_out, out=gw_out, ffn_in=gw_ffn_in,
              ffn_down=gw_ffn_down)
    small = dict(norm_mix_g=dg_mix, b_fgt=db_fgt, b_gate=db_gate, norm_ffn_g=dg_ffn, norm_final_g=dg_final)
    return loss, dx, gw, small


def _position():
    return lax.axis_index("x"), lax.axis_index("y"), lax.axis_index("c")


def _other_chips(x, y):
    return [(1 - x, y), (x, 1 - y), (1 - x, 1 - y)]


ROW_TILE = 16


def _row_chunks(rows, want=4):
    n = want
    while n > 1 and rows % (n * ROW_TILE):
        n //= 2
    return n


def _gather_weights(shards):
    n = len(shards)
    nq = max(_row_chunks(s.shape[0] // 2) for s in shards)

    def body(*refs):
        ins, outs = refs[:n], refs[n:2 * n]
        send_sems, recv_sems = refs[2 * n:]
        x, y, c = _position()
        mine = 2 * x + y
        chips = _other_chips(x, y)
        started = []

        def pieces(w, core):
            half = ins[w].shape[0] // 2
            size = half // _row_chunks(half)
            return [pl.ds(core * half + q * size, size) for q in range(_row_chunks(half))]

        for w in range(n):
            for r, (cx, cy) in enumerate(chips):
                for q, rows in enumerate(pieces(w, c)):
                    cp = pltpu.make_async_remote_copy(
                        src_ref=ins[w].at[rows, :], dst_ref=outs[w].at[mine, rows, :], send_sem=send_sems.at[w, r, q],
                        recv_sem=recv_sems.at[w, r, q], device_id=(cx, cy, c), device_id_type=MESH)
                    cp.start()
                    started.append(cp)

        def landed(w, r, q, rows, peer):
            cx, cy = chips[r % 3]
            blk = outs[w].at[2 * cx + cy, rows, :]
            return pltpu.make_async_remote_copy(src_ref=blk, dst_ref=blk, send_sem=send_sems.at[w, r, q],
                                                recv_sem=recv_sems.at[w, r, q], device_id=peer, device_id_type=MESH)

        for w in range(n):
            for r, (cx, cy) in enumerate(chips):
                for q, rows in enumerate(pieces(w, c)):
                    landed(w, r, q, rows, (cx, cy, c)).wait_recv()
                    fwd = landed(w, 3 + r, q, rows, (x, y, 1 - c))
                    fwd.start()
                    started.append(fwd)
        for w in range(n):
            for r in range(3):
                for q, rows in enumerate(pieces(w, 1 - c)):
                    landed(w, 3 + r, q, rows, (x, y, 1 - c)).wait_recv()
        for cp in started:
            cp.wait_send()

    return _pcall(
        body, name="gather_weights", in_specs=[HBM_SPEC] * n, out_specs=[HBM_SPEC] * n,
        out_shape=[jax.ShapeDtypeStruct((4,) + s.shape, s.dtype) for s in shards],
        scratch_shapes=[pltpu.SemaphoreType.DMA((n, 6, nq)), pltpu.SemaphoreType.DMA((n, 6, nq))],
    )(*shards)


SEM_SPEC = pl.BlockSpec(memory_space=pltpu.SEMAPHORE)
ANY_SPEC = pl.BlockSpec(memory_space=pl.ANY)
DATAFLOW = pltpu.SideEffectType.DATAFLOW_SIDE_EFFECTING


def _in_hbm(a):
    return pltpu.with_memory_space_constraint(a, pltpu.HBM)


def _split_copy_start(srcs, land_shapes, copies, after, *, name):
    n, m = len(srcs), len(land_shapes)

    def body(*refs):
        src_refs, land_refs = refs[:n], refs[n:n + m]
        send_sems, recv_sems = refs[n + m + 1], refs[n + m + 2]
        token = refs[-1]
        x, y, c = _position()
        for k, (src, dst, peer) in enumerate(copies(x, y, c, src_refs, land_refs)):
            pltpu.make_async_remote_copy(src_ref=src, dst_ref=dst, send_sem=send_sems.at[k], recv_sem=recv_sems.at[k],
                                         device_id=peer, device_id_type=MESH).start()
        token[...] = jnp.zeros_like(token)

    lands = [lax.empty(s.shape, s.dtype) for s in land_shapes]
    count = len(copies(0, 0, 0, srcs, lands))
    out = _pcall(
        body, name=name,
        out_shape=(pltpu.SemaphoreType.DMA((count,)), pltpu.SemaphoreType.DMA((count,)),
                   *[pltpu.HBM(s.shape, s.dtype) for s in srcs], *[pltpu.HBM(s.shape, s.dtype) for s in land_shapes],
                   jax.ShapeDtypeStruct((8, 128), F32)),
        in_specs=[HBM_SPEC] * (n + m) + [ANY_SPEC],
        out_specs=(SEM_SPEC, SEM_SPEC, *[HBM_SPEC] * (n + m), pl.BlockSpec(memory_space=pltpu.VMEM)),
        input_output_aliases={k: 2 + k for k in range(n + m)},
        compiler_params=pltpu.CompilerParams(has_side_effects=DATAFLOW),
    )(*[_in_hbm(s) for s in srcs], *[_in_hbm(l) for l in lands], after)
    return out[0], out[1], list(out[2:2 + n]), list(out[2 + n:2 + n + m]), out[-1]


def _split_copy_wait(send_sems, recv_sems, srcs, lands, copies, after, *, name):
    n, m = len(srcs), len(lands)

    def body(*refs):
        src_refs, land_refs = refs[:n], refs[n:n + m]
        send, recv = refs[n + m], refs[n + m + 1]
        x, y, c = _position()
        for k, (src, dst, peer) in enumerate(copies(x, y, c, src_refs, land_refs)):
            cp = pltpu.make_async_remote_copy(src_ref=src, dst_ref=dst, send_sem=send.at[k], recv_sem=recv.at[k],
                                              device_id=peer, device_id_type=MESH)
            cp.wait_send()
            cp.wait_recv()

    out = _pcall(
        body, name=name,
        out_shape=tuple(pltpu.HBM(s.shape, s.dtype) for s in list(srcs) + list(lands)),
        in_specs=[HBM_SPEC] * (n + m) + [SEM_SPEC, SEM_SPEC, ANY_SPEC], out_specs=tuple([HBM_SPEC] * (n + m)),
        input_output_aliases={k: k for k in range(n + m)},
        compiler_params=pltpu.CompilerParams(has_side_effects=DATAFLOW),
    )(*srcs, *lands, send_sems, recv_sems, after)
    return list(out[:n]), list(out[n:])


def _gather_copies(x, y, c, shard_refs, land_refs):
    out = []
    for s, l in zip(shard_refs, land_refs):
        half = s.shape[0] // 2
        nq = _row_chunks(half)
        for cx, cy in _other_chips(x, y):
            for q in range(nq):
                rows = pl.ds(c * half + q * (half // nq), half // nq)
                out.append((s.at[rows, :], l.at[2 * x + y, rows, :], (cx, cy, c)))
    return out


def _scatter_copies(x, y, c, part_refs, land_refs):
    out = []
    for p, l in zip(part_refs, land_refs):
        nq = _row_chunks(p.shape[1])
        for r, (cx, cy) in enumerate(_other_chips(x, y)):
            for q in range(nq):
                rows = pl.ds(q * (p.shape[1] // nq), p.shape[1] // nq)
                out.append((p.at[2 * cx + cy, rows, :], l.at[r, rows, :], (cx, cy, c)))
    return out


def _forward_halves(lands, *, name):
    n = len(lands)

    def body(*refs):
        ins = refs[:n]
        send_sems, recv_sems = refs[2 * n:]
        x, y, c = _position()
        copies = []
        for w in range(n):
            half = ins[w].shape[1] // 2
            for r, (cx, cy) in enumerate(_other_chips(x, y)):
                blk = ins[w].at[2 * cx + cy, pl.ds(c * half, half), :]
                cp = pltpu.make_async_remote_copy(src_ref=blk, dst_ref=blk, send_sem=send_sems.at[w, r],
                                                  recv_sem=recv_sems.at[w, r], device_id=(x, y, 1 - c),
                                                  device_id_type=MESH)
                cp.start()
                copies.append(cp)
        for w in range(n):
            half = ins[w].shape[1] // 2
            for r, (cx, cy) in enumerate(_other_chips(x, y)):
                blk = ins[w].at[2 * cx + cy, pl.ds((1 - c) * half, half), :]
                pltpu.make_async_remote_copy(src_ref=blk, dst_ref=blk, send_sem=send_sems.at[w, r],
                                             recv_sem=recv_sems.at[w, r], device_id=(x, y, 1 - c),
                                             device_id_type=MESH).wait_recv()
        for cp in copies:
            cp.wait_send()

    return _pcall(
        body, name=name, in_specs=[HBM_SPEC] * n, out_specs=[HBM_SPEC] * n,
        out_shape=[jax.ShapeDtypeStruct(l.shape, l.dtype) for l in lands],
        input_output_aliases={k: k for k in range(n)},
        scratch_shapes=[pltpu.SemaphoreType.DMA((n, 3)), pltpu.SemaphoreType.DMA((n, 3))],
    )(*lands)


def _swap_halves(grads, name="swap_halves"):
    n = len(grads)

    def body(*refs):
        ins, outs = refs[:n], refs[n:2 * n]
        send_sems, recv_sems = refs[2 * n:]
        x, y, c = _position()
        copies = []
        for w in range(n):
            half = ins[w].shape[1] // 2
            cp = pltpu.make_async_remote_copy(
                src_ref=ins[w].at[:, pl.ds((1 - c) * half, half), :], dst_ref=outs[w], send_sem=send_sems.at[w],
                recv_sem=recv_sems.at[w], device_id=(x, y, 1 - c), device_id_type=MESH)
            cp.start()
            copies.append(cp)
        for cp in copies:
            cp.wait()

    return _pcall(
        body, name=name, in_specs=[HBM_SPEC] * n, out_specs=[HBM_SPEC] * n,
        out_shape=[jax.ShapeDtypeStruct((4, g.shape[1] // 2, g.shape[2]), g.dtype) for g in grads],
        scratch_shapes=[pltpu.SemaphoreType.DMA((n,)), pltpu.SemaphoreType.DMA((n,))],
    )(*grads)


def _share_halves(halves):
    n = len(halves)

    def body(*refs):
        ins, outs = refs[:n], refs[n:2 * n]
        send_sems, recv_sems = refs[2 * n:]
        x, y, c = _position()
        copies = []
        for w in range(n):
            cp = pltpu.make_async_remote_copy(src_ref=ins[w], dst_ref=outs[w], send_sem=send_sems.at[w],
                                              recv_sem=recv_sems.at[w], device_id=(x, y, 1 - c), device_id_type=MESH)
            cp.start()
            copies.append(cp)
        for cp in copies:
            cp.wait()

    return _pcall(
        body, name="share_halves", in_specs=[HBM_SPEC] * n, out_specs=[HBM_SPEC] * n,
        out_shape=[jax.ShapeDtypeStruct(h.shape, h.dtype) for h in halves],
        scratch_shapes=[pltpu.SemaphoreType.DMA((n,)), pltpu.SemaphoreType.DMA((n,))],
    )(*halves)


def _sum_small(part):
    rows, width = part.shape

    def body(x_ref, out_ref, all_ref, send_sems, recv_sems):
        x, y, c = _position()
        me, sibling = (x, y, c), (x, y, 1 - c)
        chips = _other_chips(x, y)

        def block(px, py, pc):
            return all_ref.at[pl.ds((4 * px + 2 * py + pc) * rows, rows), :]

        def copy(k, blk, to, src=None):
            return pltpu.make_async_remote_copy(
                src_ref=block(*blk) if src is None else src, dst_ref=block(*blk), send_sem=send_sems.at[k],
                recv_sem=recv_sems.at[k], device_id=to, device_id_type=MESH)

        all_ref[pl.ds((4 * x + 2 * y + c) * rows, rows), :] = x_ref[...]
        first = [copy(0, me, sibling, src=x_ref)]
        first += [copy(1 + j, me, (*chip, c), src=x_ref) for j, chip in enumerate(chips)]
        for cp in first:
            cp.start()
        passed = [copy(4 + j, (*chip, c), sibling) for j, chip in enumerate(chips)]
        for j, chip in enumerate(chips):
            copy(1 + j, (*chip, c), me).wait_recv()
            passed[j].start()
        copy(0, sibling, me).wait_recv()
        for j, chip in enumerate(chips):
            copy(4 + j, (*chip, 1 - c), me).wait_recv()
        for cp in first + passed:
            cp.wait_send()
        total = all_ref[0:rows, :]
        for d in range(1, 8):
            total = total + all_ref[d * rows:(d + 1) * rows, :]
        out_ref[...] = total

    vm = pl.BlockSpec(memory_space=pltpu.VMEM)
    return _pcall(
        body, name="sum_small", in_specs=[vm], out_specs=vm, out_shape=jax.ShapeDtypeStruct((rows, width), F32),
        scratch_shapes=[pltpu.VMEM((8 * rows, width), F32), pltpu.SemaphoreType.DMA((7,)), pltpu.SemaphoreType.DMA((7,))],
    )(part)


def _row_tile(R, C, itemsize=4, budget=1 << 20):
    for t in (512, 256, 128, 64, 32, 16, 8):
        if R % t == 0 and t * C * itemsize <= budget:
            return t
    return R


def _add_halves(g, recv, c, *, name):
    _, R, C = g.shape
    half = R // 2
    t = _row_tile(half, C)
    nb = half // t

    def body(c_ref, g_ref, r_ref, o_ref):
        o_ref[...] = (g_ref[...].astype(F32) + r_ref[...].astype(F32)).astype(o_ref.dtype)

    grid_spec = pltpu.PrefetchScalarGridSpec(
        num_scalar_prefetch=1, grid=(4, nb),
        in_specs=[pl.BlockSpec((1, t, C), lambda k, i, cr: (k, cr[0] * nb + i, 0)),
                  pl.BlockSpec((1, t, C), lambda k, i, cr: (k, i, 0))],
        out_specs=pl.BlockSpec((1, t, C), lambda k, i, cr: (k, i, 0)))
    return _pcall(body, name=name, grid_spec=grid_spec, out_shape=jax.ShapeDtypeStruct((4, half, C), g.dtype),
                  compiler_params=_params("parallel", "parallel"))(c, g, recv)


def _add_owners(mine, recv, *, name):
    half, C = mine.shape
    t = _row_tile(half, C)

    def body(m_ref, r_ref, o_ref):
        o_ref[...] = ((m_ref[...].astype(F32) + r_ref[0].astype(F32)) + r_ref[1].astype(F32)) + r_ref[2].astype(F32)

    return _pcall(body, name=name, grid=(half // t,),
                  in_specs=[pl.BlockSpec((t, C), lambda i: (i, 0)), pl.BlockSpec((3, t, C), lambda i: (0, i, 0))],
                  out_specs=pl.BlockSpec((t, C), lambda i: (i, 0)), out_shape=jax.ShapeDtypeStruct((half, C), F32),
                  compiler_params=_params("parallel"))(mine, recv)


def _adamw(w, g, m, v, *, name):
    R, C = w.shape
    t = _row_tile(R, C)
    c1 = 1.0 - ADAM_B1 ** ADAM_STEP
    c2 = 1.0 - ADAM_B2 ** ADAM_STEP

    def body(w_ref, g_ref, m_ref, v_ref, d_ref, nm_ref, nv_ref):
        gv = g_ref[...]
        mn = ADAM_B1 * m_ref[...] + (1.0 - ADAM_B1) * gv
        vn = ADAM_B2 * v_ref[...] + (1.0 - ADAM_B2) * (gv * gv)
        d_ref[...] = -ADAM_LR * ((mn / c1) / (jnp.sqrt(vn / c2) + ADAM_EPS) + ADAM_WD * w_ref[...])
        nm_ref[...] = mn
        nv_ref[...] = vn

    blk = pl.BlockSpec((t, C), lambda i: (i, 0))
    shp = jax.ShapeDtypeStruct((R, C), F32)
    return _pcall(body, name=name, grid=(R // t,), in_specs=[blk] * 4, out_specs=[blk] * 3, out_shape=[shp] * 3,
                  compiler_params=_params("parallel"))(w, g, m, v)


BIG = ("w_in", "w_dil_out", "w_fox_out", "w_out", "w_ffn_in", "w_ffn_down")
SMALL = ("norm_mix_g", "b_fgt", "b_gate", "norm_ffn_g", "norm_final_g")
ORDER = ("norm_mix_g", "w_in", "b_fgt", "b_gate", "w_dil_out", "w_fox_out", "w_out", "norm_ffn_g", "w_ffn_in",
         "w_ffn_down", "norm_final_g")
SMALL_ROWS = {"norm_mix_g": (0, 1), "b_gate": (1, 3), "norm_ffn_g": (3, 4), "norm_final_g": (4, 5), "b_fgt": (5, 6)}


def _columns_to_blocks(full, ncol):
    K = full.shape[0]
    return full.reshape(K, 4, ncol).transpose(1, 0, 2)


def _blocks_to_columns(blocks):
    n, K, ncol = blocks.shape
    return blocks.transpose(1, 0, 2).reshape(K, n * ncol)


def kernel(x, norm_mix_g, w_in, b_fgt, b_gate, w_dil_out, w_fox_out, w_out, norm_ffn_g, w_ffn_in, w_ffn_down, norm_final_g, loss_target, m_norm_mix_g, m_w_in, m_b_fgt, m_b_gate, m_w_dil_out, m_w_fox_out, m_w_out, m_norm_ffn_g, m_w_ffn_in, m_w_ffn_down, m_norm_final_g, v_norm_mix_g, v_w_in, v_b_fgt, v_b_gate, v_w_dil_out, v_w_fox_out, v_w_out, v_norm_ffn_g, v_w_ffn_in, v_w_ffn_down, v_norm_final_g):
    weights = dict(norm_mix_g=norm_mix_g, w_in=w_in, b_fgt=b_fgt, b_gate=b_gate, w_dil_out=w_dil_out,
                   w_fox_out=w_fox_out, w_out=w_out, norm_ffn_g=norm_ffn_g, w_ffn_in=w_ffn_in, w_ffn_down=w_ffn_down,
                   norm_final_g=norm_final_g)
    m_in = dict(norm_mix_g=m_norm_mix_g, w_in=m_w_in, b_fgt=m_b_fgt, b_gate=m_b_gate, w_dil_out=m_w_dil_out,
                w_fox_out=m_w_fox_out, w_out=m_w_out, norm_ffn_g=m_norm_ffn_g, w_ffn_in=m_w_ffn_in,
                w_ffn_down=m_w_ffn_down, norm_final_g=m_norm_final_g)
    v_in = dict(norm_mix_g=v_norm_mix_g, w_in=v_w_in, b_fgt=v_b_fgt, b_gate=v_b_gate, w_dil_out=v_w_dil_out,
                w_fox_out=v_w_fox_out, w_out=v_w_out, norm_ffn_g=v_norm_ffn_g, w_ffn_in=v_w_ffn_in,
                w_ffn_down=v_w_ffn_down, norm_final_g=v_norm_final_g)
    c = lax.axis_index("c")
    chip = 2 * lax.axis_index("x") + lax.axis_index("y")

    shards = {n: weights[n][0].astype(_CD) for n in BIG}
    (g_in,) = _gather_weights([shards["w_in"]])
    late = BIG[1:]
    send_g, recv_g, late_src, late_land, token = _split_copy_start(
        [shards[n] for n in late], [jax.ShapeDtypeStruct((4,) + shards[n].shape, _CD) for n in late],
        _gather_copies, g_in, name="gather_late_start")
    full_in = _blocks_to_columns(lax.dynamic_update_index_in_dim(g_in, shards["w_in"], chip, 0))
    o3 = QKV_COLS
    o4 = o3 + N_FOX_HEADS
    w = dict(qkv=full_in[:, :o3], f=jnp.pad(full_in[:, o3:o4], ((0, 0), (0, F_PAD - N_FOX_HEADS))), g=full_in[:, o4:])
    p = dict(norm_mix_g=norm_mix_g, b_fgt=jnp.pad(b_fgt, ((0, 0), (0, F_PAD - N_FOX_HEADS))), b_gate=b_gate,
             norm_ffn_g=norm_ffn_g, norm_final_g=norm_final_g.reshape(1, D_MODEL))

    def late_weights(after):
        own, lands = _split_copy_wait(send_g, recv_g, late_src, late_land, _gather_copies, after,
                                      name="gather_late_wait")
        lands = _forward_halves(lands, name="gather_late_forward")
        g_dil, g_fox, g_out, g_ffn_in, g_ffn_down = [
            lax.dynamic_update_index_in_dim(l, s, chip, 0) for l, s in zip(lands, own)]
        return dict(dil_out=_blocks_to_columns(g_dil), fox_out=_blocks_to_columns(g_fox),
                    out=g_out.reshape(D_MODEL, D_MODEL), ffn_in=g_ffn_in,
                    ffn_down=g_ffn_down.reshape(D_FF, D_MODEL))

    c_arr = jnp.reshape(c, (1,)).astype(jnp.int32)

    def to_blocks(n, full):
        shape = weights[n].shape
        if full.ndim == 3:
            return full
        if n in ("w_out", "w_ffn_down"):
            return full.reshape(4, shape[1], shape[2])
        return _columns_to_blocks(full, shape[2])

    def pair_sums(group, named):
        names = list(named)
        blocks = [to_blocks(n, named[n]) for n in names]
        from_sibling = _swap_halves(blocks, name=f"swap_halves_{group}")
        return [_add_halves(b, r, c_arr, name=f"add_halves_{n}") for b, r, n in zip(blocks, from_sibling, names)]

    in_flight = {}

    def grad_sink(group, gw):
        if group == "in":
            named = {"w_in": jnp.concatenate([gw["qkv"], gw["f"][:, :N_FOX_HEADS], gw["g"]], axis=1)}
        else:
            named = {"w_" + k: v for k, v in gw.items()}
        sums = pair_sums(group, named)
        started = _split_copy_start(sums, [jax.ShapeDtypeStruct((3,) + s.shape[1:], s.dtype) for s in sums],
                                    _scatter_copies, next(iter(gw.values())), name=f"scatter_{group}_start")
        in_flight[group] = (list(named), started)
        return started[-1]

    loss_part, grad_x, gw, small = _layer_step(x[0], loss_target[0], w, p, late_weights, grad_sink, token)

    def owner_sums(names, sums, from_chips):
        return {n: _add_owners(lax.dynamic_index_in_dim(s, chip, 0, keepdims=False), r, name=f"add_owners_{n}")
                for n, s, r in zip(names, sums, from_chips)}

    halves = {}
    for group, (names, (send_s, recv_s, srcs, lands, _)) in in_flight.items():
        sums, from_chips = _split_copy_wait(send_s, recv_s, srcs, lands, _scatter_copies, grad_x,
                                            name=f"scatter_{group}_wait")
        halves.update(owner_sums(names, sums, from_chips))
    halves = [halves[n] for n in BIG]
    grads = {}
    for n, own, other in zip(BIG, halves, _share_halves(halves)):
        pair = jnp.stack([own, other])
        grads[n] = jnp.where(c == 0, pair, pair[::-1]).reshape(2 * own.shape[0], own.shape[1])

    packed = jnp.concatenate([
        small["norm_mix_g"], small["b_gate"].reshape(2, D_MODEL), small["norm_ffn_g"], small["norm_final_g"],
        jnp.pad(small["b_fgt"], ((0, 0), (0, D_MODEL - F_PAD))), jnp.zeros((2, D_MODEL), F32)], axis=0)
    summed = _sum_small(packed)
    for n in SMALL:
        lo, hi = SMALL_ROWS[n]
        grads[n] = summed[lo:hi].reshape(1, -1)[:, :weights[n].size]

    loss = lax.psum(loss_part[0, 0], ("x", "y", "c"))

    out_g, out_d, out_m, out_v = {}, {}, {}, {}
    for n in ORDER:
        shape = weights[n].shape
        two_d = shape[1:] if len(shape) == 3 else (1, weights[n].size)
        g2 = grads[n].reshape(two_d)
        d2, m2, v2 = _adamw(weights[n].reshape(two_d), g2, m_in[n].reshape(two_d), v_in[n].reshape(two_d),
                            name=f"adamw_{n}")
        out_g[n], out_d[n], out_m[n], out_v[n] = (g2.reshape(shape), d2.reshape(shape), m2.reshape(shape),
                                                  v2.reshape(shape))
    return (loss, grad_x[None], *[out_g[n] for n in ORDER], *[out_d[n] for n in ORDER],
            *[out_m[n] for n in ORDER], *[out_v[n] for n in ORDER])
```

```python
import numpy as np
import jax
import jax.numpy as jnp
from jax import lax
from jax.experimental import pallas as pl
from jax.experimental.pallas import tpu as pltpu

F32 = jnp.float32
_CD = jnp.bfloat16

D_MODEL = 1024
HEAD_DIM = 64
DIL_PAIRS = ((128, 1), (512, 4), (2048, 16))
N_DIL_GROUPS = 3
DIL_HEADS = 4
DIL_W = 128
DIL_OUT = DIL_HEADS * HEAD_DIM
DIL_WIDTH = N_DIL_GROUPS * DIL_OUT
N_FOX_HEADS = 8
FOX_WIDTH = N_FOX_HEADS * HEAD_DIM
D_FF = 2816
QKV_COLS = 3 * DIL_WIDTH + 3 * FOX_WIDTH
F_PAD = 128
RMS_EPS = 1e-6
NEG_INF = -1e30
ATTN_SCALE = HEAD_DIM ** -0.5
ADAM_LR, ADAM_B1, ADAM_B2, ADAM_EPS, ADAM_WD, ADAM_STEP = 0.001, 0.9, 0.999, 1e-08, 0.01, 10

VMEM_LIMIT = 48 * 1024 * 1024
MESH = pl.DeviceIdType.MESH
HBM_SPEC = pl.BlockSpec(memory_space=pltpu.HBM)


def _pcall(body, after=None, **kw):
    if after is None:
        return pl.pallas_call(body, **kw)
    n_in = len(kw["in_specs"])
    kw["in_specs"] = list(kw["in_specs"]) + [pl.BlockSpec(memory_space=pl.ANY)]

    def tied(*refs):
        return body(*refs[:n_in], *refs[n_in + 1:])

    call = pl.pallas_call(tied, **kw)
    return lambda *args: call(*args, after)


def _params(*sem):
    return pltpu.CompilerParams(dimension_semantics=sem, vmem_limit_bytes=VMEM_LIMIT)


def _pick(dim, pref):
    t = (min(pref, dim) // 128) * 128
    while t >= 128:
        if dim % t == 0:
            return t
        t -= 128
    return dim


def _mm(a, b, *, name, ta=False, tb=False, out_dtype=F32, add=None, tm=1024, tn=512, tk=2048, after=None,
        b_blocks=False, out_blocks=None):
    if ta:
        K, M = a.shape
    else:
        M, K = a.shape
    b_rows, b_cols = (b.shape[1], b.shape[0] * b.shape[2]) if b_blocks else b.shape
    if tb:
        N, K2 = b_rows, b_cols
    else:
        K2, N = b_rows, b_cols
    assert K == K2, (a.shape, b.shape)
    shard = b.shape[2] if b_blocks else None
    tm = _pick(M, tm)
    tn = _pick(shard if (b_blocks and not tb) else (out_blocks or N), tn)
    tk = _pick(shard if (b_blocks and tb) else K, tk)
    nk = K // tk
    dn = (((0 if ta else 1,), (1 if tb else 0,)), ((), ()))
    has_add = add is not None
    assert not (has_add and out_blocks)

    def body(*refs):
        a_ref, b_ref = refs[0], refs[1]
        add_ref = refs[2] if has_add else None
        o_ref = refs[3] if has_add else refs[2]
        bv = b_ref[0] if b_blocks else b_ref[...]
        p = lax.dot_general(a_ref[...].astype(_CD), bv.astype(_CD), dn, preferred_element_type=F32)

        def finish(r):
            if has_add:
                r = r + add_ref[...]
            if out_blocks:
                o_ref[0] = r.astype(out_dtype)
            else:
                o_ref[...] = r.astype(out_dtype)

        if nk == 1:
            finish(p)
        else:
            acc_ref = refs[-1]
            k = pl.program_id(2)

            @pl.when(k == 0)
            def _():
                acc_ref[...] = p

            @pl.when(k > 0)
            def _():
                acc_ref[...] += p

            @pl.when(k == nk - 1)
            def _():
                finish(acc_ref[...])

    a_spec = pl.BlockSpec((tk, tm), lambda i, j, k: (k, i)) if ta else pl.BlockSpec((tm, tk), lambda i, j, k: (i, k))
    if b_blocks and tb:
        per = shard // tk
        b_spec = pl.BlockSpec((1, tn, tk), lambda i, j, k: (k // per, j, k % per))
    elif b_blocks:
        per = shard // tn
        b_spec = pl.BlockSpec((1, tk, tn), lambda i, j, k: (j // per, k, j % per))
    else:
        b_spec = pl.BlockSpec((tn, tk), lambda i, j, k: (j, k)) if tb else pl.BlockSpec((tk, tn), lambda i, j, k: (k, j))
    if out_blocks:
        oper = out_blocks // tn
        o_spec = pl.BlockSpec((1, tm, tn), lambda i, j, k: (j // oper, i, j % oper))
        out_shape = jax.ShapeDtypeStruct((N // out_blocks, M, out_blocks), out_dtype)
    else:
        o_spec = pl.BlockSpec((tm, tn), lambda i, j, k: (i, j))
        out_shape = jax.ShapeDtypeStruct((M, N), out_dtype)
    in_specs = [a_spec, b_spec] + ([o_spec] if has_add else [])
    args = (a, b) + ((add,) if has_add else ())
    return _pcall(
        body, after, name=name, grid=(M // tm, N // tn, nk), in_specs=in_specs, out_specs=o_spec,
        out_shape=out_shape,
        scratch_shapes=[pltpu.VMEM((tm, tn), F32)] if nk > 1 else [],
        compiler_params=_params("parallel", "parallel", "arbitrary"),
    )(*args)


def _rms_fwd(x, g, *, name, tm=512, after=None):
    S, D = x.shape

    def body(x_ref, g_ref, h_ref):
        xv = x_ref[...]
        r = lax.rsqrt(jnp.mean(xv * xv, axis=-1, keepdims=True) + RMS_EPS)
        h_ref[...] = ((xv * r) * g_ref[...]).astype(h_ref.dtype)

    row = pl.BlockSpec((tm, D), lambda i: (i, 0))
    return _pcall(body, after, name=name, grid=(S // tm,), in_specs=[row, pl.BlockSpec((1, D), lambda i: (0, 0))],
                  out_specs=row, out_shape=jax.ShapeDtypeStruct((S, D), _CD), compiler_params=_params("parallel"))(x, g)


def _rms_bwd(x, g, dh, dres, *, name, tm=512, after=None):
    S, D = x.shape

    def body(x_ref, g_ref, dh_ref, dres_ref, dx_ref, dg_ref):
        xv = x_ref[...]
        r = lax.rsqrt(jnp.mean(xv * xv, axis=-1, keepdims=True) + RMS_EPS)
        xh = xv * r
        dhv = dh_ref[...]
        dxh = dhv * g_ref[...]
        dx_ref[...] = dres_ref[...] + r * (dxh - xh * jnp.mean(dxh * xh, axis=-1, keepdims=True))
        part = jnp.sum(dhv * xh, axis=0, keepdims=True)

        @pl.when(pl.program_id(0) == 0)
        def _():
            dg_ref[...] = part

        @pl.when(pl.program_id(0) > 0)
        def _():
            dg_ref[...] += part

    row = pl.BlockSpec((tm, D), lambda i: (i, 0))
    vec = pl.BlockSpec((1, D), lambda i: (0, 0))
    return _pcall(body, after, name=name, grid=(S // tm,), in_specs=[row, vec, row, row], out_specs=[row, vec],
                  out_shape=[jax.ShapeDtypeStruct((S, D), F32), jax.ShapeDtypeStruct((1, D), F32)],
                  compiler_params=_params("arbitrary"))(x, g, dh, dres)


def _loss_head(x, g, tgt, *, name, tm=512):
    S, D = x.shape

    def body(x_ref, g_ref, t_ref, loss_ref, dx_ref, dg_ref):
        xv = x_ref[...]
        gv = g_ref[...]
        r = lax.rsqrt(jnp.mean(xv * xv, axis=-1, keepdims=True) + RMS_EPS)
        xh = xv * r
        err = xh * gv - t_ref[...]
        lpart = 0.5 * jnp.sum(jnp.mean(err * err, axis=-1, keepdims=True), axis=0, keepdims=True)
        dy = err * (1.0 / D)
        dxh = dy * gv
        dx_ref[...] = r * (dxh - xh * jnp.mean(dxh * xh, axis=-1, keepdims=True))
        gpart = jnp.sum(dy * xh, axis=0, keepdims=True)

        @pl.when(pl.program_id(0) == 0)
        def _():
            loss_ref[...] = lpart
            dg_ref[...] = gpart

        @pl.when(pl.program_id(0) > 0)
        def _():
            loss_ref[...] += lpart
            dg_ref[...] += gpart

    row = pl.BlockSpec((tm, D), lambda i: (i, 0))
    vec = pl.BlockSpec((1, D), lambda i: (0, 0))
    one = pl.BlockSpec((1, 1), lambda i: (0, 0))
    return _pcall(body, name=name, grid=(S // tm,), in_specs=[row, vec, row], out_specs=[one, row, vec],
                  out_shape=[jax.ShapeDtypeStruct((1, 1), F32), jax.ShapeDtypeStruct((S, D), F32),
                             jax.ShapeDtypeStruct((1, D), F32)],
                  compiler_params=_params("arbitrary"))(x, g, tgt)


def _sigmoid(z):
    return 1.0 / (1.0 + jnp.exp(-z))


def _gate_fwd(gl, bg, ya, yb, *, name, tm=512):
    S, D = ya.shape

    def body(za_ref, zb_ref, ba_ref, bb_ref, ya_ref, yb_ref, o_ref):
        ga = _sigmoid(za_ref[...] + ba_ref[...])
        gb = _sigmoid(zb_ref[...] + bb_ref[...])
        o_ref[...] = (ga * ya_ref[...] + gb * yb_ref[...]).astype(o_ref.dtype)

    lo = pl.BlockSpec((tm, D), lambda i: (i, 0))
    hi = pl.BlockSpec((tm, D), lambda i: (i, 1))
    vlo = pl.BlockSpec((1, D), lambda i: (0, 0))
    vhi = pl.BlockSpec((1, D), lambda i: (0, 1))
    return _pcall(body, name=name, grid=(S // tm,), in_specs=[lo, hi, vlo, vhi, lo, lo], out_specs=lo,
                  out_shape=jax.ShapeDtypeStruct((S, D), _CD), compiler_params=_params("parallel"))(gl, gl, bg, bg, ya, yb)


def _gate_bwd(dm, gl, bg, ya, yb, *, name, tm=256):
    S, D = ya.shape

    def body(dm_ref, za_ref, zb_ref, ba_ref, bb_ref, ya_ref, yb_ref, dya_ref, dyb_ref, dgl_ref, dbg_ref):
        dmv = dm_ref[...]
        ga = _sigmoid(za_ref[...] + ba_ref[...])
        gb = _sigmoid(zb_ref[...] + bb_ref[...])
        dya_ref[...] = (dmv * ga).astype(dya_ref.dtype)
        dyb_ref[...] = (dmv * gb).astype(dyb_ref.dtype)
        dza = dmv * ya_ref[...] * ga * (1.0 - ga)
        dzb = dmv * yb_ref[...] * gb * (1.0 - gb)
        dgl_ref[:, :D] = dza.astype(dgl_ref.dtype)
        dgl_ref[:, D:] = dzb.astype(dgl_ref.dtype)
        pa = jnp.sum(dza, axis=0, keepdims=True)
        pb = jnp.sum(dzb, axis=0, keepdims=True)

        @pl.when(pl.program_id(0) == 0)
        def _():
            dbg_ref[:, :D] = pa
            dbg_ref[:, D:] = pb

        @pl.when(pl.program_id(0) > 0)
        def _():
            dbg_ref[:, :D] += pa
            dbg_ref[:, D:] += pb

    lo = pl.BlockSpec((tm, D), lambda i: (i, 0))
    hi = pl.BlockSpec((tm, D), lambda i: (i, 1))
    vlo = pl.BlockSpec((1, D), lambda i: (0, 0))
    vhi = pl.BlockSpec((1, D), lambda i: (0, 1))
    wide = pl.BlockSpec((tm, 2 * D), lambda i: (i, 0))
    vwide = pl.BlockSpec((1, 2 * D), lambda i: (0, 0))
    return _pcall(body, name=name, grid=(S // tm,), in_specs=[lo, lo, hi, vlo, vhi, lo, lo],
                  out_specs=[lo, lo, wide, vwide],
                  out_shape=[jax.ShapeDtypeStruct((S, D), _CD), jax.ShapeDtypeStruct((S, D), _CD),
                             jax.ShapeDtypeStruct((S, 2 * D), _CD), jax.ShapeDtypeStruct((1, 2 * D), F32)],
                  compiler_params=_params("arbitrary"))(dm, gl, gl, bg, bg, ya, yb)


def _swiglu_fwd(gu, *, name, tm=256):
    S, F2 = gu.shape
    F = F2 // 2

    def body(g_ref, u_ref, o_ref):
        gv = g_ref[...]
        o_ref[...] = (gv * _sigmoid(gv) * u_ref[...]).astype(o_ref.dtype)

    lo = pl.BlockSpec((tm, F), lambda i: (i, 0))
    hi = pl.BlockSpec((tm, F), lambda i: (i, 1))
    return _pcall(body, name=name, grid=(S // tm,), in_specs=[lo, hi], out_specs=lo,
                  out_shape=jax.ShapeDtypeStruct((S, F), _CD), compiler_params=_params("parallel"))(gu, gu)


def _swiglu_bwd(dact, gu, *, name, tm=256):
    S, F2 = gu.shape
    F = F2 // 2

    def body(d_ref, g_ref, u_ref, o_ref):
        dv = d_ref[...]
        gv = g_ref[...]
        sg = _sigmoid(gv)
        o_ref[:, :F] = (dv * u_ref[...] * (sg * (1.0 + gv * (1.0 - sg)))).astype(o_ref.dtype)
        o_ref[:, F:] = (dv * (gv * sg)).astype(o_ref.dtype)

    lo = pl.BlockSpec((tm, F), lambda i: (i, 0))
    hi = pl.BlockSpec((tm, F), lambda i: (i, 1))
    return _pcall(body, name=name, grid=(S // tm,), in_specs=[lo, lo, hi],
                  out_specs=pl.BlockSpec((tm, F2), lambda i: (i, 0)),
                  out_shape=jax.ShapeDtypeStruct((S, F2), _CD), compiler_params=_params("parallel"))(dact, gu, gu)


def _split3(x):
    hi = x.astype(jnp.bfloat16)
    r1 = x - hi.astype(F32)
    mid = r1.astype(jnp.bfloat16)
    lo = (r1 - mid.astype(F32)).astype(jnp.bfloat16)
    return hi, mid, lo


def _ones_dot_left(ones, x):
    return sum(jnp.dot(ones, p, preferred_element_type=F32) for p in _split3(x))


def _ones_dot_right(x, ones):
    return sum(jnp.dot(p, ones, preferred_element_type=F32) for p in _split3(x))


def _head_sum(x):
    n = x.shape[1]
    r = lax.broadcasted_iota(jnp.int32, (n, n), 0) // HEAD_DIM
    c = lax.broadcasted_iota(jnp.int32, (n, n), 1) // HEAD_DIM
    return _ones_dot_right(x, (r == c).astype(jnp.bfloat16))


def _log_sigmoid(z):
    e = jnp.exp(-jnp.abs(z))
    t = 1.0 + e
    log1p_e = jnp.where(t == 1.0, e, jnp.log(t) * (e / jnp.where(t == 1.0, 1.0, t - 1.0)))
    return jnp.minimum(z, 0.0) - log1p_e


def _fox_cumsum(zf, bf, *, name):
    S, W = zf.shape
    nb = S // 128

    def body(z_ref, b_ref, c_ref):
        tri = (lax.broadcasted_iota(jnp.int32, (128, 128), 0) >= lax.broadcasted_iota(jnp.int32, (128, 128), 1))
        tri = tri.astype(jnp.bfloat16)

        def step(i, carry):
            rows = pl.ds(pl.multiple_of(i * 128, 128), 128)
            lf = _log_sigmoid(z_ref[rows, :] + b_ref[...])
            cb = _ones_dot_left(tri, lf) + carry
            c_ref[rows, :] = cb
            return cb[127:128, :]

        lax.fori_loop(0, nb, step, jnp.zeros((1, W), F32))

    return _pcall(body, name=name, out_shape=jax.ShapeDtypeStruct((S, W), F32),
                  compiler_params=pltpu.CompilerParams(vmem_limit_bytes=VMEM_LIMIT))(zf, bf)


def _fox_cumsum_bwd(dc, zf, bf, *, name):
    S, W = zf.shape
    nb = S // 128

    def body(dc_ref, z_ref, b_ref, dz_ref, db_ref):
        tri = (lax.broadcasted_iota(jnp.int32, (128, 128), 0) <= lax.broadcasted_iota(jnp.int32, (128, 128), 1))
        tri = tri.astype(jnp.bfloat16)

        def step(k, carry):
            tail, acc = carry
            i = nb - 1 - k
            rows = pl.ds(pl.multiple_of(i * 128, 128), 128)
            dlf = _ones_dot_left(tri, dc_ref[rows, :]) + tail
            dz = dlf * _sigmoid(-(z_ref[rows, :] + b_ref[...]))
            dz_ref[rows, :] = dz
            return dlf[0:1, :], acc + jnp.sum(dz, axis=0, keepdims=True)

        _, acc = lax.fori_loop(0, nb, step, (jnp.zeros((1, W), F32), jnp.zeros((1, W), F32)))
        db_ref[...] = acc

    return _pcall(body, name=name,
                  out_shape=[jax.ShapeDtypeStruct((S, W), F32), jax.ShapeDtypeStruct((1, W), F32)],
                  compiler_params=pltpu.CompilerParams(vmem_limit_bytes=VMEM_LIMIT))(dc, zf, bf)


def _dil_slopes(group):
    h = np.arange(1, N_DIL_GROUPS * DIL_HEADS + 1, dtype=np.float32)
    s = (np.float32(2.0) ** (np.float32(-8.0) * h / np.float32(N_DIL_GROUPS * DIL_HEADS))).astype(np.float32)
    return [float(v) for v in s.reshape(N_DIL_GROUPS, DIL_HEADS)[group]]


def _dil_window(first):
    kj = lax.broadcasted_iota(jnp.int32, (2 * DIL_W, DIL_W), 0)
    qi = lax.broadcasted_iota(jnp.int32, (2 * DIL_W, DIL_W), 1)
    seen = jnp.logical_and(kj >= qi, kj <= qi + DIL_W)
    return jnp.logical_and(seen, jnp.logical_or(kj >= DIL_W, jnp.logical_not(first)))


QKV_BLOCKS = QKV_COLS // DIL_OUT


def _dil_chunk(S, dilation):
    return min(4 * DIL_W, S // dilation)


def _dil_fwd(qkv, group, *, name):
    S = qkv.shape[0]
    dilation = DIL_PAIRS[group][1]
    sub = S // dilation
    chunk = _dil_chunk(S, dilation)
    nb = chunk // DIL_W
    sld = [s * dilation for s in _dil_slopes(group)]
    nkeys = chunk + DIL_W
    nt = (((1,), (1,)), ((), ()))
    tn = (((0,), (0,)), ((), ()))

    def body(q_ref, k_ref, v_ref, kp_ref, vp_ref, o_ref, l_ref, qs, ks, vs):
        n = pl.program_id(1)

        @pl.when(jnp.logical_and(pl.program_id(0) == 0, n == 0))
        def _():
            qpos = (lax.broadcasted_iota(jnp.int32, (chunk, 1), 0) + DIL_W).astype(F32)
            kpos = lax.broadcasted_iota(jnp.int32, (nkeys, 1), 0).astype(F32)
            for h in range(DIL_HEADS):
                hi = slice(h * PACK + HEAD_DIM, (h + 1) * PACK)
                qs[:, hi] = _extras(_pieces(-sld[h] * qpos), ONES3, chunk).astype(qs.dtype)
                ks[:, hi] = _extras(ONES3, _pieces(sld[h] * kpos), nkeys).astype(ks.dtype)
                vs[:, hi] = _extras(ONES3, ZEROS3, nkeys).astype(vs.dtype)

        for h in range(DIL_HEADS):
            src = slice(h * HEAD_DIM, (h + 1) * HEAD_DIM)
            lo = slice(h * PACK, h * PACK + HEAD_DIM)
            qs[:, lo] = (q_ref[:, src].astype(F32) * ATTN_SCALE).astype(qs.dtype)
            ks[:DIL_W, lo] = kp_ref[:, src]
            ks[DIL_W:, lo] = k_ref[:, src]
            vs[:DIL_W, lo] = vp_ref[:, src]
            vs[DIL_W:, lo] = v_ref[:, src]
        for i in range(nb):
            ok = _dil_window(n * nb + i == 0)
            rows_q = slice(i * DIL_W, (i + 1) * DIL_W)
            rows_k = slice(i * DIL_W, (i + 2) * DIL_W)
            for h in range(DIL_HEADS):
                cols = slice(h * PACK, (h + 1) * PACK)
                dst = slice(h * HEAD_DIM, (h + 1) * HEAD_DIM)
                st = lax.dot_general(ks[rows_k, cols], qs[rows_q, cols], nt, preferred_element_type=F32)
                st = jnp.where(ok, st, NEG_INF)
                m = jnp.max(st, axis=0, keepdims=True)
                pt = jnp.exp(st - m)
                acc = lax.dot_general(vs[rows_k, cols], pt.astype(_CD), tn, preferred_element_type=F32)
                den = acc[HEAD_DIM:HEAD_DIM + 1, :]
                o_ref[rows_q, dst] = (acc[:HEAD_DIM, :] / den).T
                l_ref[rows_q, dst] = jnp.broadcast_to(m + jnp.log(den), (HEAD_DIM, DIL_W)).T

    def cur(col0):
        return pl.BlockSpec((chunk, DIL_OUT), lambda r, n: (n, QKV_BLOCKS * r + col0 + group))

    def prev(col0):
        return pl.BlockSpec((DIL_W, DIL_OUT), lambda r, n: (jnp.maximum(nb * n - 1, 0), QKV_BLOCKS * r + col0 + group))

    out = pl.BlockSpec((chunk, DIL_OUT), lambda r, n: (n, r))
    view = qkv.reshape(sub, dilation * QKV_COLS)
    shp = jax.ShapeDtypeStruct((sub, dilation * DIL_OUT), F32)
    o, lse = _pcall(body, name=name, grid=(dilation, sub // chunk),
                    in_specs=[cur(0), cur(3), cur(6), prev(3), prev(6)], out_specs=[out, out], out_shape=[shp, shp],
                    scratch_shapes=[pltpu.VMEM((chunk, DIL_HEADS * PACK), _CD), pltpu.VMEM((nkeys, DIL_HEADS * PACK), _CD),
                                    pltpu.VMEM((nkeys, DIL_HEADS * PACK), _CD)],
                    compiler_params=_params("arbitrary", "arbitrary"))(view, view, view, view, view)
    return o.reshape(S, DIL_OUT), lse.reshape(S, DIL_OUT)


def _dil_bwd(qkv, o, lse, do, dlse, group, *, name):
    S = qkv.shape[0]
    dilation = DIL_PAIRS[group][1]
    sub = S // dilation
    chunk = _dil_chunk(S, dilation)
    nb = chunk // DIL_W
    sld = [s * dilation for s in _dil_slopes(group)]
    nkeys = chunk + DIL_W
    nchunk = sub // chunk
    width = DIL_HEADS * PACK
    nt = (((1,), (1,)), ((), ()))
    tn = (((0,), (0,)), ((), ()))

    def body(q_ref, k_ref, v_ref, kp_ref, vp_ref, o_ref, l_ref, do_ref, dl_ref, dq_ref, dk_ref, dv_ref,
             qs, ks, vs, dos, dk_s, dv_s):
        step = pl.program_id(1)
        n = nchunk - 1 - step

        @pl.when(jnp.logical_and(pl.program_id(0) == 0, step == 0))
        def _():
            kpos = lax.broadcasted_iota(jnp.int32, (nkeys, 1), 0).astype(F32)
            for h in range(DIL_HEADS):
                hi = slice(h * PACK + HEAD_DIM, (h + 1) * PACK)
                ks[:, hi] = _extras(ONES3, _pieces(sld[h] * kpos), nkeys).astype(ks.dtype)
                vs[:, hi] = _extras(ONES3, ZEROS3, nkeys).astype(vs.dtype)

        @pl.when(step == 0)
        def _():
            dk_s[chunk:, :] = jnp.zeros((DIL_W, width), F32)
            dv_s[chunk:, :] = jnp.zeros((DIL_W, width), F32)

        dk_s[:chunk, :] = jnp.zeros((chunk, width), F32)
        dv_s[:chunk, :] = jnp.zeros((chunk, width), F32)
        qpos = (lax.broadcasted_iota(jnp.int32, (chunk, 1), 0) + DIL_W).astype(F32)
        for h in range(DIL_HEADS):
            src = slice(h * HEAD_DIM, (h + 1) * HEAD_DIM)
            lo = slice(h * PACK, h * PACK + HEAD_DIM)
            hi = slice(h * PACK + HEAD_DIM, (h + 1) * PACK)
            one = slice(h * HEAD_DIM, h * HEAD_DIM + 1)
            qs[:, lo] = (q_ref[:, src].astype(F32) * ATTN_SCALE).astype(qs.dtype)
            qs[:, hi] = _extras(_pieces(-sld[h] * qpos - l_ref[:, one]), ONES3, chunk).astype(qs.dtype)
            ks[:DIL_W, lo] = kp_ref[:, src]
            ks[DIL_W:, lo] = k_ref[:, src]
            vs[:DIL_W, lo] = vp_ref[:, src]
            vs[DIL_W:, lo] = v_ref[:, src]
            dov = do_ref[:, src]
            dsum = jnp.sum(dov * o_ref[:, src], axis=-1, keepdims=True)
            dos[:, lo] = dov.astype(dos.dtype)
            dos[:, hi] = _extras(_pieces(dl_ref[:, one] - dsum), ZEROS3, chunk).astype(dos.dtype)
        for i in range(nb):
            ok = _dil_window(n * nb + i == 0)
            rows_q = slice(i * DIL_W, (i + 1) * DIL_W)
            rows_k = slice(i * DIL_W, (i + 2) * DIL_W)
            for h in range(DIL_HEADS):
                cols = slice(h * PACK, (h + 1) * PACK)
                dst = slice(h * HEAD_DIM, (h + 1) * HEAD_DIM)
                qv, kv, vv, dov = qs[rows_q, cols], ks[rows_k, cols], vs[rows_k, cols], dos[rows_q, cols]
                pt = jnp.exp(lax.dot_general(kv, qv, nt, preferred_element_type=F32))
                pt = jnp.where(ok, pt, 0.0)
                dlt = pt * lax.dot_general(vv, dov, nt, preferred_element_type=F32)
                dlb = dlt.astype(_CD)
                dv_s[rows_k, cols] += jnp.dot(pt.astype(_CD), dov, preferred_element_type=F32)
                dk_s[rows_k, cols] += jnp.dot(dlb, qv, preferred_element_type=F32)
                dq = lax.dot_general(dlb, kv, tn, preferred_element_type=F32)
                dq_ref[rows_q, dst] = (dq[:, :HEAD_DIM] * ATTN_SCALE).astype(dq_ref.dtype)
        for h in range(DIL_HEADS):
            lo = slice(h * PACK, h * PACK + HEAD_DIM)
            dst = slice(h * HEAD_DIM, (h + 1) * HEAD_DIM)
            dk_ref[:, dst] = dk_s[DIL_W:, lo].astype(dk_ref.dtype)
            dv_ref[:, dst] = dv_s[DIL_W:, lo].astype(dv_ref.dtype)
        dk_s[chunk:, :] = dk_s[:DIL_W, :]
        dv_s[chunk:, :] = dv_s[:DIL_W, :]

    def cur(col0):
        return pl.BlockSpec((chunk, DIL_OUT), lambda r, s: (nchunk - 1 - s, QKV_BLOCKS * r + col0 + group))

    def prev(col0):
        return pl.BlockSpec((DIL_W, DIL_OUT),
                            lambda r, s: (jnp.maximum(nb * (nchunk - 1 - s) - 1, 0), QKV_BLOCKS * r + col0 + group))

    nat = pl.BlockSpec((chunk, DIL_OUT), lambda r, s: (nchunk - 1 - s, r))
    view = qkv.reshape(sub, dilation * QKV_COLS)
    views = [t.reshape(sub, dilation * DIL_OUT) for t in (o, lse, do, dlse)]
    shp = jax.ShapeDtypeStruct((sub, dilation * DIL_OUT), _CD)
    grads = _pcall(body, name=name, grid=(dilation, nchunk),
                   in_specs=[cur(0), cur(3), cur(6), prev(3), prev(6), nat, nat, nat, nat],
                   out_specs=[nat, nat, nat], out_shape=[shp, shp, shp],
                   scratch_shapes=[pltpu.VMEM((chunk, width), _CD), pltpu.VMEM((nkeys, width), _CD),
                                   pltpu.VMEM((nkeys, width), _CD), pltpu.VMEM((chunk, width), _CD),
                                   pltpu.VMEM((nkeys, width), F32), pltpu.VMEM((nkeys, width), F32)],
                   compiler_params=_params("arbitrary", "arbitrary"))(view, view, view, view, view, *views)
    return [t.reshape(S, DIL_OUT) for t in grads]


def _dil_mix_fwd(os_, ls_, *, name, tm=512):
    S, W = os_[0].shape

    def body(o0, o1, o2, l0, l1, l2, out_ref):
        ls = [l0[...], l1[...], l2[...]]
        m = jnp.maximum(jnp.maximum(ls[0], ls[1]), ls[2])
        es = [jnp.exp(l - m) for l in ls]
        den = es[0] + es[1] + es[2]
        out_ref[...] = ((es[0] * o0[...] + es[1] * o1[...] + es[2] * o2[...]) / den).astype(out_ref.dtype)

    row = pl.BlockSpec((tm, W), lambda i: (i, 0))
    return _pcall(body, name=name, grid=(S // tm,), in_specs=[row] * 6, out_specs=row,
                  out_shape=jax.ShapeDtypeStruct((S, W), _CD), compiler_params=_params("parallel"))(*os_, *ls_)


def _dil_mix_bwd(doa, os_, ls_, *, name, tm=512, after=None):
    S, W = doa.shape

    def body(d_ref, o0, o1, o2, l0, l1, l2, do0, do1, do2, dl0, dl1, dl2):
        dv = d_ref[...]
        ls = [l0[...], l1[...], l2[...]]
        m = jnp.maximum(jnp.maximum(ls[0], ls[1]), ls[2])
        es = [jnp.exp(l - m) for l in ls]
        den = es[0] + es[1] + es[2]
        al = [e / den for e in es]
        da = [_head_sum(dv * o[...]) for o in (o0, o1, o2)]
        mean = al[0] * da[0] + al[1] * da[1] + al[2] * da[2]
        for a, d_, do_ref, dl_ref in zip(al, da, (do0, do1, do2), (dl0, dl1, dl2)):
            do_ref[...] = a * dv
            dl_ref[...] = a * (d_ - mean)

    row = pl.BlockSpec((tm, W), lambda i: (i, 0))
    shp = jax.ShapeDtypeStruct((S, W), F32)
    return _pcall(body, after, name=name, grid=(S // tm,), in_specs=[row] * 7, out_specs=[row] * 6, out_shape=[shp] * 6,
                  compiler_params=_params("parallel"))(doa, *os_, *ls_)


FOX_T = 512


PACK = 2 * HEAD_DIM
HEAD_PAIRS = N_FOX_HEADS // 2
Q_BLOCK0 = (3 * DIL_WIDTH) // PACK
K_BLOCK0 = (3 * DIL_WIDTH + FOX_WIDTH) // PACK
V_BLOCK0 = (3 * DIL_WIDTH + 2 * FOX_WIDTH) // PACK


def _pieces(x):
    hi = x.astype(jnp.bfloat16).astype(F32)
    r = x - hi
    mid = r.astype(jnp.bfloat16).astype(F32)
    lo = (r - mid).astype(jnp.bfloat16).astype(F32)
    return [hi, mid, lo]


def _extras(first, second, rows):
    lane = lax.broadcasted_iota(jnp.int32, (rows, HEAD_DIM), 1)
    out = jnp.zeros((rows, HEAD_DIM), F32)
    for idx, val in enumerate(list(first) + list(second)):
        out = jnp.where(lane == idx, val, out)
    return out


def _head_column(c, h):
    lane = lax.broadcasted_iota(jnp.int32, c.shape, 1)
    return jnp.sum(jnp.where(lane == h, c, 0.0), axis=1, keepdims=True)


ONES3 = [1.0, 1.0, 1.0]
ZEROS3 = [0.0, 0.0, 0.0]


def _fox_pack_fwd(qkv, c, *, name, tm=512):
    S = qkv.shape[0]

    def body(q_ref, k_ref, v_ref, c_ref, qo_ref, ko_ref, vo_ref):
        hp = pl.program_id(1)
        cv = c_ref[...]
        for hh in range(2):
            ch = _pieces(_head_column(cv, 2 * hp + hh))
            src = slice(hh * HEAD_DIM, (hh + 1) * HEAD_DIM)
            lo = slice(hh * PACK, hh * PACK + HEAD_DIM)
            hi = slice(hh * PACK + HEAD_DIM, (hh + 1) * PACK)
            qo_ref[:, lo] = (q_ref[:, src].astype(F32) * ATTN_SCALE).astype(qo_ref.dtype)
            qo_ref[:, hi] = _extras(ch, ONES3, tm).astype(qo_ref.dtype)
            ko_ref[:, lo] = k_ref[:, src]
            ko_ref[:, hi] = _extras(ONES3, [-p for p in ch], tm).astype(ko_ref.dtype)
            vo_ref[:, lo] = v_ref[:, src]
            vo_ref[:, hi] = _extras(ONES3, ZEROS3, tm).astype(vo_ref.dtype)

    def src(block0):
        return pl.BlockSpec((tm, PACK), lambda i, hp: (i, block0 + hp))

    out = pl.BlockSpec((tm, 2 * PACK), lambda i, hp: (i, hp))
    shp = jax.ShapeDtypeStruct((S, N_FOX_HEADS * PACK), _CD)
    return _pcall(body, name=name, grid=(S // tm, HEAD_PAIRS),
                  in_specs=[src(Q_BLOCK0), src(K_BLOCK0), src(V_BLOCK0), pl.BlockSpec((tm, PACK), lambda i, hp: (i, 0))],
                  out_specs=[out, out, out], out_shape=[shp, shp, shp],
                  compiler_params=_params("parallel", "parallel"))(qkv, qkv, qkv, c)


def _fox_fwd(qp, kp, vp, *, name):
    S = qp.shape[0]
    nt = S // FOX_T
    nt_dims = (((1,), (1,)), ((), ()))
    tn_dims = (((0,), (0,)), ((), ()))

    def body(i_tab, j_tab, q_ref, k_ref, v_ref, o_ref, l_ref, m_s, acc_s):
        t = pl.program_id(1)
        i, j = i_tab[t], j_tab[t]

        @pl.when(j == 0)
        def _():
            m_s[...] = jnp.full((2, 1, FOX_T), NEG_INF, F32)
            acc_s[...] = jnp.zeros((2, PACK, FOX_T), F32)

        def tile(diagonal):
            for hh in range(2):
                cols = slice(hh * PACK, (hh + 1) * PACK)
                st = lax.dot_general(k_ref[:, cols], q_ref[:, cols], nt_dims, preferred_element_type=F32)
                if diagonal:
                    key = lax.broadcasted_iota(jnp.int32, (FOX_T, FOX_T), 0)
                    qry = lax.broadcasted_iota(jnp.int32, (FOX_T, FOX_T), 1)
                    st = jnp.where(key <= qry, st, NEG_INF)
                m_old = m_s[hh]
                m_new = jnp.maximum(m_old, jnp.max(st, axis=0, keepdims=True))
                pt = jnp.exp(st - m_new)
                acc_s[hh] = jnp.exp(m_old - m_new) * acc_s[hh] + lax.dot_general(
                    v_ref[:, cols], pt.astype(_CD), tn_dims, preferred_element_type=F32)
                m_s[hh] = m_new

        @pl.when(j < i)
        def _():
            tile(False)

        @pl.when(j == i)
        def _():
            tile(True)
            for hh in range(2):
                acc = acc_s[hh]
                den = acc[HEAD_DIM:HEAD_DIM + 1, :]
                cols = slice(hh * HEAD_DIM, (hh + 1) * HEAD_DIM)
                o_ref[:, cols] = (acc[:HEAD_DIM, :] / den).T
                l_ref[:, cols] = jnp.broadcast_to(m_s[hh] + jnp.log(den), (HEAD_DIM, FOX_T)).T

    pairs = [(i, j) for i in range(nt) for j in range(i + 1)]
    i_tab = jnp.asarray([p[0] for p in pairs], jnp.int32)
    j_tab = jnp.asarray([p[1] for p in pairs], jnp.int32)
    qs = pl.BlockSpec((FOX_T, 2 * PACK), lambda hp, t, it, jt: (it[t], hp))
    ks = pl.BlockSpec((FOX_T, 2 * PACK), lambda hp, t, it, jt: (jt[t], hp))
    os_ = pl.BlockSpec((FOX_T, PACK), lambda hp, t, it, jt: (it[t], hp))
    shp = jax.ShapeDtypeStruct((S, FOX_WIDTH), F32)
    grid_spec = pltpu.PrefetchScalarGridSpec(
        num_scalar_prefetch=2, grid=(HEAD_PAIRS, len(pairs)), in_specs=[qs, ks, ks], out_specs=[os_, os_],
        scratch_shapes=[pltpu.VMEM((2, 1, FOX_T), F32), pltpu.VMEM((2, PACK, FOX_T), F32)])
    return _pcall(body, name=name, grid_spec=grid_spec, out_shape=[shp, shp],
                  compiler_params=_params("parallel", "arbitrary"))(i_tab, j_tab, qp, kp, vp)


def _fox_pack_bwd(qkv, c, o, lse, do, *, name, tm=512, after=None):
    S = qkv.shape[0]

    def body(q_ref, c_ref, o_ref, l_ref, do_ref, qo_ref, do_out_ref):
        hp = pl.program_id(1)
        cv = c_ref[...]
        for hh in range(2):
            src = slice(hh * HEAD_DIM, (hh + 1) * HEAD_DIM)
            lo = slice(hh * PACK, hh * PACK + HEAD_DIM)
            hi = slice(hh * PACK + HEAD_DIM, (hh + 1) * PACK)
            shift = _head_column(cv, 2 * hp + hh) - l_ref[:, hh * HEAD_DIM:hh * HEAD_DIM + 1]
            dov = do_ref[:, src]
            dsum = jnp.sum(dov * o_ref[:, src], axis=-1, keepdims=True)
            qo_ref[:, lo] = (q_ref[:, src].astype(F32) * ATTN_SCALE).astype(qo_ref.dtype)
            qo_ref[:, hi] = _extras(_pieces(shift), ONES3, tm).astype(qo_ref.dtype)
            do_out_ref[:, lo] = dov.astype(do_out_ref.dtype)
            do_out_ref[:, hi] = _extras(_pieces(-dsum), ZEROS3, tm).astype(do_out_ref.dtype)

    pair = pl.BlockSpec((tm, PACK), lambda i, hp: (i, hp))
    out = pl.BlockSpec((tm, 2 * PACK), lambda i, hp: (i, hp))
    shp = jax.ShapeDtypeStruct((S, N_FOX_HEADS * PACK), _CD)
    return _pcall(body, after, name=name, grid=(S // tm, HEAD_PAIRS),
                  in_specs=[pl.BlockSpec((tm, PACK), lambda i, hp: (i, Q_BLOCK0 + hp)),
                            pl.BlockSpec((tm, PACK), lambda i, hp: (i, 0)), pair, pair, pair],
                  out_specs=[out, out], out_shape=[shp, shp],
                  compiler_params=_params("parallel", "parallel"))(qkv, c, o, lse, do)


def _fox_bwd(qp, kp, vp, dop, *, name):
    S = qp.shape[0]
    nt = S // FOX_T
    nt_dims = (((1,), (1,)), ((), ()))
    tn_dims = (((0,), (0,)), ((), ()))

    def body(i_tab, j_tab, q_ref, k_ref, v_ref, do_ref, dq_ref, dk_ref, dv_ref, dc_ref, dr_ref,
             dq_s, dk_s, dv_s, dc_s, dr_s):
        t = pl.program_id(1)
        i, j = i_tab[t], j_tab[t]

        @pl.when(t == 0)
        def _():
            dq_s[...] = jnp.zeros((S, 2 * PACK), F32)
            dr_s[...] = jnp.zeros((2, 1, S), F32)

        @pl.when(i == j)
        def _():
            dk_s[...] = jnp.zeros((FOX_T, 2 * PACK), F32)
            dv_s[...] = jnp.zeros((FOX_T, 2 * PACK), F32)
            dc_s[...] = jnp.zeros((2, FOX_T, 1), F32)

        def tile(diagonal):
            rows = pl.ds(pl.multiple_of(i * FOX_T, FOX_T), FOX_T)
            for hh in range(2):
                cols = slice(hh * PACK, (hh + 1) * PACK)
                qv, kv, vv, dov = q_ref[:, cols], k_ref[:, cols], v_ref[:, cols], do_ref[:, cols]
                pt = jnp.exp(lax.dot_general(kv, qv, nt_dims, preferred_element_type=F32))
                if diagonal:
                    key = lax.broadcasted_iota(jnp.int32, (FOX_T, FOX_T), 0)
                    qry = lax.broadcasted_iota(jnp.int32, (FOX_T, FOX_T), 1)
                    pt = jnp.where(key <= qry, pt, 0.0)
                dst = pt * lax.dot_general(vv, dov, nt_dims, preferred_element_type=F32)
                dsb = dst.astype(_CD)
                dc_s[hh] += jnp.sum(dst, axis=1, keepdims=True)
                dr_s[hh, :, rows] += jnp.sum(dst, axis=0, keepdims=True)
                dv_s[:, cols] += jnp.dot(pt.astype(_CD), dov, preferred_element_type=F32)
                dk_s[:, cols] += jnp.dot(dsb, qv, preferred_element_type=F32)
                dq_s[rows, cols] += lax.dot_general(dsb, kv, tn_dims, preferred_element_type=F32)

        @pl.when(i > j)
        def _():
            tile(False)

        @pl.when(i == j)
        def _():
            tile(True)

        @pl.when(i == nt - 1)
        def _():
            for hh in range(2):
                src = slice(hh * PACK, hh * PACK + HEAD_DIM)
                dst_cols = slice(hh * HEAD_DIM, (hh + 1) * HEAD_DIM)
                dk_ref[:, dst_cols] = dk_s[:, src].astype(dk_ref.dtype)
                dv_ref[:, dst_cols] = dv_s[:, src].astype(dv_ref.dtype)
                dc_ref[:, dst_cols] = jnp.broadcast_to(dc_s[hh], (FOX_T, HEAD_DIM))

        @pl.when(t == len(pairs) - 1)
        def _():
            for hh in range(2):
                dq_ref[:, hh * HEAD_DIM:(hh + 1) * HEAD_DIM] = (
                    dq_s[:, hh * PACK:hh * PACK + HEAD_DIM] * ATTN_SCALE).astype(dq_ref.dtype)
            dr_ref[...] = dr_s[...]

    pairs = [(i, j) for j in range(nt) for i in range(j, nt)]
    i_tab = jnp.asarray([p[0] for p in pairs], jnp.int32)
    j_tab = jnp.asarray([p[1] for p in pairs], jnp.int32)
    qs = pl.BlockSpec((FOX_T, 2 * PACK), lambda hp, t, it, jt: (it[t], hp))
    ks = pl.BlockSpec((FOX_T, 2 * PACK), lambda hp, t, it, jt: (jt[t], hp))
    whole = pl.BlockSpec((S, PACK), lambda hp, t, it, jt: (0, hp))
    cs = pl.BlockSpec((FOX_T, PACK), lambda hp, t, it, jt: (jt[t], hp))
    rs = pl.BlockSpec((2, 1, S), lambda hp, t, it, jt: (hp, 0, 0))
    shp = jax.ShapeDtypeStruct((S, FOX_WIDTH), _CD)
    grid_spec = pltpu.PrefetchScalarGridSpec(
        num_scalar_prefetch=2, grid=(HEAD_PAIRS, len(pairs)), in_specs=[qs, ks, ks, qs],
        out_specs=[whole, cs, cs, cs, rs],
        scratch_shapes=[pltpu.VMEM((S, 2 * PACK), F32), pltpu.VMEM((FOX_T, 2 * PACK), F32),
                        pltpu.VMEM((FOX_T, 2 * PACK), F32), pltpu.VMEM((2, FOX_T, 1), F32),
                        pltpu.VMEM((2, 1, S), F32)])
    return _pcall(body, name=name, grid_spec=grid_spec,
                  out_shape=[shp, shp, shp, jax.ShapeDtypeStruct((S, FOX_WIDTH), F32),
                             jax.ShapeDtypeStruct((N_FOX_HEADS, 1, S), F32)],
                  compiler_params=_params("parallel", "arbitrary"))(i_tab, j_tab, qp, kp, vp, dop)


def _layer_step(x, tgt, w, p, late_weights=None, grad_sink=None, after=None):
    S = x.shape[0]
    h = _rms_fwd(x, p["norm_mix_g"], name="rms_mix")
    qkv = _mm(h, w["qkv"], name="proj_qkv", out_dtype=_CD, tn=768, after=after)
    zf = _mm(h, w["f"], name="proj_f")
    gl = _mm(h, w["g"], name="proj_gate", tn=1024)

    dil_o, dil_l = [], []
    for g in range(N_DIL_GROUPS):
        og, lg = _dil_fwd(qkv, g, name=f"dil_fwd{g}")
        dil_o.append(og), dil_l.append(lg)
    o_a = _dil_mix_fwd(dil_o, dil_l, name="dil_mix")

    c = _fox_cumsum(zf, p["b_fgt"], name="fox_cumsum")
    fqp, fkp, fvp = _fox_pack_fwd(qkv, c, name="fox_pack")
    o_b, flse = _fox_fwd(fqp, fkp, fvp, name="fox_fwd")

    if late_weights is not None:
        w = {**w, **late_weights(o_b)}
    y_a = _mm(o_a, w["dil_out"], name="y_a", tn=1024)
    y_b = _mm(o_b, w["fox_out"], name="y_b", tn=1024)
    merged = _gate_fwd(gl, p["b_gate"], y_a, y_b, name="gate_fwd")
    x1 = _mm(merged, w["out"], name="mix_out", add=x)

    h2 = _rms_fwd(x1, p["norm_ffn_g"], name="rms_ffn")
    gu = _mm(h2, w["ffn_in"], name="ffn_in", tn=1408, b_blocks=True)
    act = _swiglu_fwd(gu, name="swiglu")
    x2 = _mm(act, w["ffn_down"], name="ffn_down", add=x1, tk=2816)

    loss, dx2, dg_final = _loss_head(x2, p["norm_final_g"], tgt, name="loss_head")

    dact = _mm(dx2, w["ffn_down"], name="d_act", tb=True, tn=1408)
    gw_ffn_down = _mm(act, dx2, name="gw_ffn_down", ta=True, out_dtype=_CD, tm=1408)
    dgu = _swiglu_bwd(dact, gu, name="swiglu_bwd")
    dh2 = _mm(dgu, w["ffn_in"], name="d_h2", tb=True, tk=1408, b_blocks=True)
    gw_ffn_in = _mm(h2, dgu, name="gw_ffn_in", ta=True, out_dtype=_CD, tn=1408, out_blocks=1408)
    sink = grad_sink if grad_sink is not None else (lambda group, grads: None)
    tok = sink("ffn", dict(ffn_in=gw_ffn_in, ffn_down=gw_ffn_down))
    dx1, dg_ffn = _rms_bwd(x1, p["norm_ffn_g"], dh2, dx2, name="rms_ffn_bwd", after=tok)

    dmerged = _mm(dx1, w["out"], name="d_merged", tb=True)
    gw_out = _mm(merged, dx1, name="gw_out", ta=True, out_dtype=_CD)
    dy_a, dy_b, dgl, db_gate = _gate_bwd(dmerged, gl, p["b_gate"], y_a, y_b, name="gate_bwd")
    do_a = _mm(dy_a, w["dil_out"], name="d_o_a", tb=True)
    gw_dil_out = _mm(o_a, dy_a, name="gw_dil_out", ta=True, out_dtype=_CD, tn=1024)
    do_b = _mm(dy_b, w["fox_out"], name="d_o_b", tb=True)
    gw_fox_out = _mm(o_b, dy_b, name="gw_fox_out", ta=True, out_dtype=_CD, tn=1024)
    tok = sink("mix", dict(dil_out=gw_dil_out, fox_out=gw_fox_out, out=gw_out))

    bqp, bdop = _fox_pack_bwd(qkv, c, o_b, flse, do_b, name="fox_pack_bwd", after=tok)
    dqp, dkp, dvp, dck, dcq = _fox_bwd(bqp, fkp, fvp, bdop, name="fox_bwd")
    dc = dcq[:, 0, :].T - dck.reshape(S, N_FOX_HEADS, HEAD_DIM)[:, :, 0]
    dc = jnp.pad(dc, ((0, 0), (0, F_PAD - N_FOX_HEADS)))
    dzf, db_fgt = _fox_cumsum_bwd(dc, zf, p["b_fgt"], name="fox_cumsum_bwd")

    douts = _dil_mix_bwd(do_a, dil_o, dil_l, name="dil_mix_bwd", after=tok)
    dqs, dks, dvs = [], [], []
    for g in range(N_DIL_GROUPS):
        dq, dk, dv = _dil_bwd(qkv, dil_o[g], dil_l[g], douts[g], douts[3 + g], g, name=f"dil_bwd{g}")
        dqs.append(dq), dks.append(dk), dvs.append(dv)
    dqkv = jnp.concatenate(dqs + dks + dvs + [dqp, dkp, dvp], axis=1)

    gw_qkv = _mm(h, dqkv, name="gw_qkv", ta=True, out_dtype=_CD, tn=768)
    gw_g = _mm(h, dgl, name="gw_gate", ta=True, out_dtype=_CD)
    gw_f = _mm(h, dzf, name="gw_f", ta=True, out_dtype=_CD)
    tok = sink("in", dict(qkv=gw_qkv, f=gw_f, g=gw_g))
    dh = _mm(dqkv, w["qkv"], name="d_h_qkv", tb=True, tk=1920, after=tok)
    dh = _mm(dgl, w["g"], name="d_h_gate", tb=True, add=dh)
    dh = _mm(dzf, w["f"], name="d_h_f", tb=True, add=dh)
    dx, dg_mix = _rms_bwd(x, p["norm_mix_g"], dh, dx1, name="rms_mix_bwd")

    gw = dict(qkv=gw_qkv, f=gw_f, g=gw_g, dil_out=gw_dil_out, fox_out=gw_fox_out, out=gw_out, ffn_in=gw_ffn_in,
              ffn_down=gw_ffn_down)
    small = dict(norm_mix_g=dg_mix, b_fgt=db_fgt, b_gate=db_gate, norm_ffn_g=dg_ffn, norm_final_g=dg_final)
    return loss, dx, gw, small


def _position():
    return lax.axis_index("x"), lax.axis_index("y"), lax.axis_index("c")


def _other_chips(x, y):
    return [(1 - x, y), (x, 1 - y), (1 - x, 1 - y)]


ROW_TILE = 16


def _row_chunks(rows, want=4):
    n = want
    while n > 1 and rows % (n * ROW_TILE):
        n //= 2
    return n


def _gather_weights(shards):
    n = len(shards)
    nq = max(_row_chunks(s.shape[0] // 2) for s in shards)

    def body(*refs):
        ins, outs = refs[:n], refs[n:2 * n]
        send_sems, recv_sems = refs[2 * n:]
        x, y, c = _position()
        mine = 2 * x + y
        chips = _other_chips(x, y)
        started = []

        def pieces(w, core):
            half = ins[w].shape[0] // 2
            size = half // _row_chunks(half)
            return [pl.ds(core * half + q * size, size) for q in range(_row_chunks(half))]

        for w in range(n):
            for r, (cx, cy) in enumerate(chips):
                for q, rows in enumerate(pieces(w, c)):
                    cp = pltpu.make_async_remote_copy(
                        src_ref=ins[w].at[rows, :], dst_ref=outs[w].at[mine, rows, :], send_sem=send_sems.at[w, r, q],
                        recv_sem=recv_sems.at[w, r, q], device_id=(cx, cy, c), device_id_type=MESH)
                    cp.start()
                    started.append(cp)

        def landed(w, r, q, rows, peer):
            cx, cy = chips[r % 3]
            blk = outs[w].at[2 * cx + cy, rows, :]
            return pltpu.make_async_remote_copy(src_ref=blk, dst_ref=blk, send_sem=send_sems.at[w, r, q],
                                                recv_sem=recv_sems.at[w, r, q], device_id=peer, device_id_type=MESH)

        for w in range(n):
            for r, (cx, cy) in enumerate(chips):
                for q, rows in enumerate(pieces(w, c)):
                    landed(w, r, q, rows, (cx, cy, c)).wait_recv()
                    fwd = landed(w, 3 + r, q, rows, (x, y, 1 - c))
                    fwd.start()
                    started.append(fwd)
        for w in range(n):
            for r in range(3):
                for q, rows in enumerate(pieces(w, 1 - c)):
                    landed(w, 3 + r, q, rows, (x, y, 1 - c)).wait_recv()
        for cp in started:
            cp.wait_send()

    return _pcall(
        body, name="gather_weights", in_specs=[HBM_SPEC] * n, out_specs=[HBM_SPEC] * n,
        out_shape=[jax.ShapeDtypeStruct((4,) + s.shape, s.dtype) for s in shards],
        scratch_shapes=[pltpu.SemaphoreType.DMA((n, 6, nq)), pltpu.SemaphoreType.DMA((n, 6, nq))],
    )(*shards)


SEM_SPEC = pl.BlockSpec(memory_space=pltpu.SEMAPHORE)
ANY_SPEC = pl.BlockSpec(memory_space=pl.ANY)
DATAFLOW = pltpu.SideEffectType.DATAFLOW_SIDE_EFFECTING


def _in_hbm(a):
    return pltpu.with_memory_space_constraint(a, pltpu.HBM)


def _split_copy_start(srcs, land_shapes, copies, after, *, name):
    n, m = len(srcs), len(land_shapes)

    def body(*refs):
        src_refs, land_refs = refs[:n], refs[n:n + m]
        send_sems, recv_sems = refs[n + m + 1], refs[n + m + 2]
        token = refs[-1]
        x, y, c = _position()
        for k, (src, dst, peer) in enumerate(copies(x, y, c, src_refs, land_refs)):
            pltpu.make_async_remote_copy(src_ref=src, dst_ref=dst, send_sem=send_sems.at[k], recv_sem=recv_sems.at[k],
                                         device_id=peer, device_id_type=MESH).start()
        token[...] = jnp.zeros_like(token)

    lands = [lax.empty(s.shape, s.dtype) for s in land_shapes]
    count = len(copies(0, 0, 0, srcs, lands))
    out = _pcall(
        body, name=name,
        out_shape=(pltpu.SemaphoreType.DMA((count,)), pltpu.SemaphoreType.DMA((count,)),
                   *[pltpu.HBM(s.shape, s.dtype) for s in srcs], *[pltpu.HBM(s.shape, s.dtype) for s in land_shapes],
                   jax.ShapeDtypeStruct((8, 128), F32)),
        in_specs=[HBM_SPEC] * (n + m) + [ANY_SPEC],
        out_specs=(SEM_SPEC, SEM_SPEC, *[HBM_SPEC] * (n + m), pl.BlockSpec(memory_space=pltpu.VMEM)),
        input_output_aliases={k: 2 + k for k in range(n + m)},
        compiler_params=pltpu.CompilerParams(has_side_effects=DATAFLOW),
    )(*[_in_hbm(s) for s in srcs], *[_in_hbm(l) for l in lands], after)
    return out[0], out[1], list(out[2:2 + n]), list(out[2 + n:2 + n + m]), out[-1]


def _split_copy_wait(send_sems, recv_sems, srcs, lands, copies, after, *, name):
    n, m = len(srcs), len(lands)

    def body(*refs):
        src_refs, land_refs = refs[:n], refs[n:n + m]
        send, recv = refs[n + m], refs[n + m + 1]
        x, y, c = _position()
        for k, (src, dst, peer) in enumerate(copies(x, y, c, src_refs, land_refs)):
            cp = pltpu.make_async_remote_copy(src_ref=src, dst_ref=dst, send_sem=send.at[k], recv_sem=recv.at[k],
                                              device_id=peer, device_id_type=MESH)
            cp.wait_send()
            cp.wait_recv()

    out = _pcall(
        body, name=name,
        out_shape=tuple(pltpu.HBM(s.shape, s.dtype) for s in list(srcs) + list(lands)),
        in_specs=[HBM_SPEC] * (n + m) + [SEM_SPEC, SEM_SPEC, ANY_SPEC], out_specs=tuple([HBM_SPEC] * (n + m)),
        input_output_aliases={k: k for k in range(n + m)},
        compiler_params=pltpu.CompilerParams(has_side_effects=DATAFLOW),
    )(*srcs, *lands, send_sems, recv_sems, after)
    return list(out[:n]), list(out[n:])


def _gather_copies(x, y, c, shard_refs, land_refs):
    out = []
    for s, l in zip(shard_refs, land_refs):
        half = s.shape[0] // 2
        nq = _row_chunks(half)
        for cx, cy in _other_chips(x, y):
            for q in range(nq):
                rows = pl.ds(c * half + q * (half // nq), half // nq)
                out.append((s.at[rows, :], l.at[2 * x + y, rows, :], (cx, cy, c)))
    return out


def _scatter_copies(x, y, c, part_refs, land_refs):
    out = []
    for p, l in zip(part_refs, land_refs):
        nq = _row_chunks(p.shape[1])
        for r, (cx, cy) in enumerate(_other_chips(x, y)):
            for q in range(nq):
                rows = pl.ds(q * (p.shape[1] // nq), p.shape[1] // nq)
                out.append((p.at[2 * cx + cy, rows, :], l.at[r, rows, :], (cx, cy, c)))
    return out


def _forward_halves(lands, *, name):
    n = len(lands)

    def body(*refs):
        ins = refs[:n]
        send_sems, recv_sems = refs[2 * n:]
        x, y, c = _position()
        copies = []
        for w in range(n):
            half = ins[w].shape[1] // 2
            for r, (cx, cy) in enumerate(_other_chips(x, y)):
                blk = ins[w].at[2 * cx + cy, pl.ds(c * half, half), :]
                cp = pltpu.make_async_remote_copy(src_ref=blk, dst_ref=blk, send_sem=send_sems.at[w, r],
                                                  recv_sem=recv_sems.at[w, r], device_id=(x, y, 1 - c),
                                                  device_id_type=MESH)
                cp.start()
                copies.append(cp)
        for w in range(n):
            half = ins[w].shape[1] // 2
            for r, (cx, cy) in enumerate(_other_chips(x, y)):
                blk = ins[w].at[2 * cx + cy, pl.ds((1 - c) * half, half), :]
                pltpu.make_async_remote_copy(src_ref=blk, dst_ref=blk, send_sem=send_sems.at[w, r],
                                             recv_sem=recv_sems.at[w, r], device_id=(x, y, 1 - c),
                                             device_id_type=MESH).wait_recv()
        for cp in copies:
            cp.wait_send()

    return _pcall(
        body, name=name, in_specs=[HBM_SPEC] * n, out_specs=[HBM_SPEC] * n,
        out_shape=[jax.ShapeDtypeStruct(l.shape, l.dtype) for l in lands],
        input_output_aliases={k: k for k in range(n)},
        scratch_shapes=[pltpu.SemaphoreType.DMA((n, 3)), pltpu.SemaphoreType.DMA((n, 3))],
    )(*lands)


def _swap_halves(grads, name="swap_halves"):
    n = len(grads)

    def body(*refs):
        ins, outs = refs[:n], refs[n:2 * n]
        send_sems, recv_sems = refs[2 * n:]
        x, y, c = _position()
        copies = []
        for w in range(n):
            half = ins[w].shape[1] // 2
            cp = pltpu.make_async_remote_copy(
                src_ref=ins[w].at[:, pl.ds((1 - c) * half, half), :], dst_ref=outs[w], send_sem=send_sems.at[w],
                recv_sem=recv_sems.at[w], device_id=(x, y, 1 - c), device_id_type=MESH)
            cp.start()
            copies.append(cp)
        for cp in copies:
            cp.wait()

    return _pcall(
        body, name=name, in_specs=[HBM_SPEC] * n, out_specs=[HBM_SPEC] * n,
        out_shape=[jax.ShapeDtypeStruct((4, g.shape[1] // 2, g.shape[2]), g.dtype) for g in grads],
        scratch_shapes=[pltpu.SemaphoreType.DMA((n,)), pltpu.SemaphoreType.DMA((n,))],
    )(*grads)


def _share_halves(halves):
    n = len(halves)

    def body(*refs):
        ins, outs = refs[:n], refs[n:2 * n]
        send_sems, recv_sems = refs[2 * n:]
        x, y, c = _position()
        copies = []
        for w in range(n):
            cp = pltpu.make_async_remote_copy(src_ref=ins[w], dst_ref=outs[w], send_sem=send_sems.at[w],
                                              recv_sem=recv_sems.at[w], device_id=(x, y, 1 - c), device_id_type=MESH)
            cp.start()
            copies.append(cp)
        for cp in copies:
            cp.wait()

    return _pcall(
        body, name="share_halves", in_specs=[HBM_SPEC] * n, out_specs=[HBM_SPEC] * n,
        out_shape=[jax.ShapeDtypeStruct(h.shape, h.dtype) for h in halves],
        scratch_shapes=[pltpu.SemaphoreType.DMA((n,)), pltpu.SemaphoreType.DMA((n,))],
    )(*halves)


def _sum_small(part):
    rows, width = part.shape

    def body(x_ref, out_ref, all_ref, send_sems, recv_sems):
        x, y, c = _position()
        me, sibling = (x, y, c), (x, y, 1 - c)
        chips = _other_chips(x, y)

        def block(px, py, pc):
            return all_ref.at[pl.ds((4 * px + 2 * py + pc) * rows, rows), :]

        def copy(k, blk, to, src=None):
            return pltpu.make_async_remote_copy(
                src_ref=block(*blk) if src is None else src, dst_ref=block(*blk), send_sem=send_sems.at[k],
                recv_sem=recv_sems.at[k], device_id=to, device_id_type=MESH)

        all_ref[pl.ds((4 * x + 2 * y + c) * rows, rows), :] = x_ref[...]
        first = [copy(0, me, sibling, src=x_ref)]
        first += [copy(1 + j, me, (*chip, c), src=x_ref) for j, chip in enumerate(chips)]
        for cp in first:
            cp.start()
        passed = [copy(4 + j, (*chip, c), sibling) for j, chip in enumerate(chips)]
        for j, chip in enumerate(chips):
            copy(1 + j, (*chip, c), me).wait_recv()
            passed[j].start()
        copy(0, sibling, me).wait_recv()
        for j, chip in enumerate(chips):
            copy(4 + j, (*chip, 1 - c), me).wait_recv()
        for cp in first + passed:
            cp.wait_send()
        total = all_ref[0:rows, :]
        for d in range(1, 8):
            total = total + all_ref[d * rows:(d + 1) * rows, :]
        out_ref[...] = total

    vm = pl.BlockSpec(memory_space=pltpu.VMEM)
    return _pcall(
        body, name="sum_small", in_specs=[vm], out_specs=vm, out_shape=jax.ShapeDtypeStruct((rows, width), F32),
        scratch_shapes=[pltpu.VMEM((8 * rows, width), F32), pltpu.SemaphoreType.DMA((7,)), pltpu.SemaphoreType.DMA((7,))],
    )(part)


def _row_tile(R, C, itemsize=4, budget=1 << 20):
    for t in (512, 256, 128, 64, 32, 16, 8):
        if R % t == 0 and t * C * itemsize <= budget:
            return t
    return R


def _add_halves(g, recv, c, *, name):
    _, R, C = g.shape
    half = R // 2
    t = _row_tile(half, C)
    nb = half // t

    def body(c_ref, g_ref, r_ref, o_ref):
        o_ref[...] = (g_ref[...].astype(F32) + r_ref[...].astype(F32)).astype(o_ref.dtype)

    grid_spec = pltpu.PrefetchScalarGridSpec(
        num_scalar_prefetch=1, grid=(4, nb),
        in_specs=[pl.BlockSpec((1, t, C), lambda k, i, cr: (k, cr[0] * nb + i, 0)),
                  pl.BlockSpec((1, t, C), lambda k, i, cr: (k, i, 0))],
        out_specs=pl.BlockSpec((1, t, C), lambda k, i, cr: (k, i, 0)))
    return _pcall(body, name=name, grid_spec=grid_spec, out_shape=jax.ShapeDtypeStruct((4, half, C), g.dtype),
                  compiler_params=_params("parallel", "parallel"))(c, g, recv)


def _add_owners(mine, recv, *, name):
    half, C = mine.shape
    t = _row_tile(half, C)

    def body(m_ref, r_ref, o_ref):
        o_ref[...] = ((m_ref[...].astype(F32) + r_ref[0].astype(F32)) + r_ref[1].astype(F32)) + r_ref[2].astype(F32)

    return _pcall(body, name=name, grid=(half // t,),
                  in_specs=[pl.BlockSpec((t, C), lambda i: (i, 0)), pl.BlockSpec((3, t, C), lambda i: (0, i, 0))],
                  out_specs=pl.BlockSpec((t, C), lambda i: (i, 0)), out_shape=jax.ShapeDtypeStruct((half, C), F32),
                  compiler_params=_params("parallel"))(mine, recv)


def _adamw(w, g, m, v, *, name):
    R, C = w.shape
    t = _row_tile(R, C)
    c1 = 1.0 - ADAM_B1 ** ADAM_STEP
    c2 = 1.0 - ADAM_B2 ** ADAM_STEP

    def body(w_ref, g_ref, m_ref, v_ref, d_ref, nm_ref, nv_ref):
        gv = g_ref[...]
        mn = ADAM_B1 * m_ref[...] + (1.0 - ADAM_B1) * gv
        vn = ADAM_B2 * v_ref[...] + (1.0 - ADAM_B2) * (gv * gv)
        d_ref[...] = -ADAM_LR * ((mn / c1) / (jnp.sqrt(vn / c2) + ADAM_EPS) + ADAM_WD * w_ref[...])
        nm_ref[...] = mn
        nv_ref[...] = vn

    blk = pl.BlockSpec((t, C), lambda i: (i, 0))
    shp = jax.ShapeDtypeStruct((R, C), F32)
    return _pcall(body, name=name, grid=(R // t,), in_specs=[blk] * 4, out_specs=[blk] * 3, out_shape=[shp] * 3,
                  compiler_params=_params("parallel"))(w, g, m, v)


BIG = ("w_in", "w_dil_out", "w_fox_out", "w_out", "w_ffn_in", "w_ffn_down")
SMALL = ("norm_mix_g", "b_fgt", "b_gate", "norm_ffn_g", "norm_final_g")
ORDER = ("norm_mix_g", "w_in", "b_fgt", "b_gate", "w_dil_out", "w_fox_out", "w_out", "norm_ffn_g", "w_ffn_in",
         "w_ffn_down", "norm_final_g")
SMALL_ROWS = {"norm_mix_g": (0, 1), "b_gate": (1, 3), "norm_ffn_g": (3, 4), "norm_final_g": (4, 5), "b_fgt": (5, 6)}


def _columns_to_blocks(full, ncol):
    K = full.shape[0]
    return full.reshape(K, 4, ncol).transpose(1, 0, 2)


def _blocks_to_columns(blocks):
    n, K, ncol = blocks.shape
    return blocks.transpose(1, 0, 2).reshape(K, n * ncol)


def kernel(x, norm_mix_g, w_in, b_fgt, b_gate, w_dil_out, w_fox_out, w_out, norm_ffn_g, w_ffn_in, w_ffn_down, norm_final_g, loss_target, m_norm_mix_g, m_w_in, m_b_fgt, m_b_gate, m_w_dil_out, m_w_fox_out, m_w_out, m_norm_ffn_g, m_w_ffn_in, m_w_ffn_down, m_norm_final_g, v_norm_mix_g, v_w_in, v_b_fgt, v_b_gate, v_w_dil_out, v_w_fox_out, v_w_out, v_norm_ffn_g, v_w_ffn_in, v_w_ffn_down, v_norm_final_g):
    weights = dict(norm_mix_g=norm_mix_g, w_in=w_in, b_fgt=b_fgt, b_gate=b_gate, w_dil_out=w_dil_out,
                   w_fox_out=w_fox_out, w_out=w_out, norm_ffn_g=norm_ffn_g, w_ffn_in=w_ffn_in, w_ffn_down=w_ffn_down,
                   norm_final_g=norm_final_g)
    m_in = dict(norm_mix_g=m_norm_mix_g, w_in=m_w_in, b_fgt=m_b_fgt, b_gate=m_b_gate, w_dil_out=m_w_dil_out,
                w_fox_out=m_w_fox_out, w_out=m_w_out, norm_ffn_g=m_norm_ffn_g, w_ffn_in=m_w_ffn_in,
                w_ffn_down=m_w_ffn_down, norm_final_g=m_norm_final_g)
    v_in = dict(norm_mix_g=v_norm_mix_g, w_in=v_w_in, b_fgt=v_b_fgt, b_gate=v_b_gate, w_dil_out=v_w_dil_out,
                w_fox_out=v_w_fox_out, w_out=v_w_out, norm_ffn_g=v_norm_ffn_g, w_ffn_in=v_w_ffn_in,
                w_ffn_down=v_w_ffn_down, norm_final_g=v_norm_final_g)
    c = lax.axis_index("c")
    chip = 2 * lax.axis_index("x") + lax.axis_index("y")

    shards = {n: weights[n][0].astype(_CD) for n in BIG}
    (g_in,) = _gather_weights([shards["w_in"]])
    late = BIG[1:]
    send_g, recv_g, late_src, late_land, token = _split_copy_start(
        [shards[n] for n in late], [jax.ShapeDtypeStruct((4,) + shards[n].shape, _CD) for n in late],
        _gather_copies, g_in, name="gather_late_start")
    full_in = _blocks_to_columns(lax.dynamic_update_index_in_dim(g_in, shards["w_in"], chip, 0))
    o3 = QKV_COLS
    o4 = o3 + N_FOX_HEADS
    w = dict(qkv=full_in[:, :o3], f=jnp.pad(full_in[:, o3:o4], ((0, 0), (0, F_PAD - N_FOX_HEADS))), g=full_in[:, o4:])
    p = dict(norm_mix_g=norm_mix_g, b_fgt=jnp.pad(b_fgt, ((0, 0), (0, F_PAD - N_FOX_HEADS))), b_gate=b_gate,
             norm_ffn_g=norm_ffn_g, norm_final_g=norm_final_g.reshape(1, D_MODEL))

    def late_weights(after):
        own, lands = _split_copy_wait(send_g, recv_g, late_src, late_land, _gather_copies, after,
                                      name="gather_late_wait")
        lands = _forward_halves(lands, name="gather_late_forward")
        g_dil, g_fox, g_out, g_ffn_in, g_ffn_down = [
            lax.dynamic_update_index_in_dim(l, s, chip, 0) for l, s in zip(lands, own)]
        return dict(dil_out=_blocks_to_columns(g_dil), fox_out=_blocks_to_columns(g_fox),
                    out=g_out.reshape(D_MODEL, D_MODEL), ffn_in=g_ffn_in,
                    ffn_down=g_ffn_down.reshape(D_FF, D_MODEL))

    c_arr = jnp.reshape(c, (1,)).astype(jnp.int32)

    def to_blocks(n, full):
        shape = weights[n].shape
        if full.ndim == 3:
            return full
        if n in ("w_out", "w_ffn_down"):
            return full.reshape(4, shape[1], shape[2])
        return _columns_to_blocks(full, shape[2])

    def pair_sums(group, named):
        names = list(named)
        blocks = [to_blocks(n, named[n]) for n in names]
        from_sibling = _swap_halves(blocks, name=f"swap_halves_{group}")
        return [_add_halves(b, r, c_arr, name=f"add_halves_{n}") for b, r, n in zip(blocks, from_sibling, names)]

    in_flight = {}

    def grad_sink(group, gw):
        if group == "in":
            named = {"w_in": jnp.concatenate([gw["qkv"], gw["f"][:, :N_FOX_HEADS], gw["g"]], axis=1)}
        else:
            named = {"w_" + k: v for k, v in gw.items()}
        sums = pair_sums(group, named)
        started = _split_copy_start(sums, [jax.ShapeDtypeStruct((3,) + s.shape[1:], s.dtype) for s in sums],
                                    _scatter_copies, next(iter(gw.values())), name=f"scatter_{group}_start")
        in_flight[group] = (list(named), started)
        return started[-1]

    loss_part, grad_x, gw, small = _layer_step(x[0], loss_target[0], w, p, late_weights, grad_sink, token)

    def owner_sums(names, sums, from_chips):
        return {n: _add_owners(lax.dynamic_index_in_dim(s, chip, 0, keepdims=False), r, name=f"add_owners_{n}")
                for n, s, r in zip(names, sums, from_chips)}

    halves = {}
    for group, (names, (send_s, recv_s, srcs, lands, _)) in in_flight.items():
        sums, from_chips = _split_copy_wait(send_s, recv_s, srcs, lands, _scatter_copies, grad_x,
                                            name=f"scatter_{group}_wait")
        halves.update(owner_sums(names, sums, from_chips))
    halves = [halves[n] for n in BIG]
    grads = {}
    for n, own, other in zip(BIG, halves, _share_halves(halves)):
        pair = jnp.stack([own, other])
        grads[n] = jnp.where(c == 0, pair, pair[::-1]).reshape(2 * own.shape[0], own.shape[1])

    packed = jnp.concatenate([
        small["norm_mix_g"], small["b_gate"].reshape(2, D_MODEL), small["norm_ffn_g"], small["norm_final_g"],
        jnp.pad(small["b_fgt"], ((0, 0), (0, D_MODEL - F_PAD))), jnp.zeros((2, D_MODEL), F32)], axis=0)
    summed = _sum_small(packed)
    for n in SMALL:
        lo, hi = SMALL_ROWS[n]
        grads[n] = summed[lo:hi].reshape(1, -1)[:, :weights[n].size]

    loss = lax.psum(loss_part[0, 0], ("x", "y", "c"))

    out_g, out_d, out_m, out_v = {}, {}, {}, {}
    for n in ORDER:
        shape = weights[n].shape
        two_d = shape[1:] if len(shape) == 3 else (1, weights[n].size)
        g2 = grads[n].reshape(two_d)
        d2, m2, v2 = _adamw(weights[n].reshape(two_d), g2, m_in[n].reshape(two_d), v_in[n].reshape(two_d),
                            name=f"adamw_{n}")
        out_g[n], out_d[n], out_m[n], out_v[n] = (g2.reshape(shape), d2.reshape(shape), m2.reshape(shape),
                                                  v2.reshape(shape))
    return (loss, grad_x[None], *[out_g[n] for n in ORDER], *[out_d[n] for n in ORDER],
            *[out_m[n] for n in ORDER], *[out_v[n] for n in ORDER])
```

```python
import numpy as np
import jax
import jax.numpy as jnp
from jax import lax
from jax.experimental import pallas as pl
from jax.experimental.pallas import tpu as pltpu

F32 = jnp.float32
_CD = jnp.bfloat16

D_MODEL = 1024
HEAD_DIM = 64
DIL_PAIRS = ((128, 1), (512, 4), (2048, 16))
N_DIL_GROUPS = 3
DIL_HEADS = 4
DIL_W = 128
DIL_OUT = DIL_HEADS * HEAD_DIM
DIL_WIDTH = N_DIL_GROUPS * DIL_OUT
N_FOX_HEADS = 8
FOX_WIDTH = N_FOX_HEADS * HEAD_DIM
D_FF = 2816
QKV_COLS = 3 * DIL_WIDTH + 3 * FOX_WIDTH
F_PAD = 128
RMS_EPS = 1e-6
NEG_INF = -1e30
ATTN_SCALE = HEAD_DIM ** -0.5
ADAM_LR, ADAM_B1, ADAM_B2, ADAM_EPS, ADAM_WD, ADAM_STEP = 0.001, 0.9, 0.999, 1e-08, 0.01, 10

VMEM_LIMIT = 48 * 1024 * 1024
MESH = pl.DeviceIdType.MESH
HBM_SPEC = pl.BlockSpec(memory_space=pltpu.HBM)


def _pcall(body, after=None, **kw):
    if after is None:
        return pl.pallas_call(body, **kw)
    n_in = len(kw["in_specs"])
    kw["in_specs"] = list(kw["in_specs"]) + [pl.BlockSpec(memory_space=pl.ANY)]

    def tied(*refs):
        return body(*refs[:n_in], *refs[n_in + 1:])

    call = pl.pallas_call(tied, **kw)
    return lambda *args: call(*args, after)


def _params(*sem):
    return pltpu.CompilerParams(dimension_semantics=sem, vmem_limit_bytes=VMEM_LIMIT)


def _pick(dim, pref):
    t = (min(pref, dim) // 128) * 128
    while t >= 128:
        if dim % t == 0:
            return t
        t -= 128
    return dim


def _mm(a, b, *, name, ta=False, tb=False, out_dtype=F32, add=None, tm=1024, tn=512, tk=2048, after=None,
        b_blocks=False, out_blocks=None):
    if ta:
        K, M = a.shape
    else:
        M, K = a.shape
    b_rows, b_cols = (b.shape[1], b.shape[0] * b.shape[2]) if b_blocks else b.shape
    if tb:
        N, K2 = b_rows, b_cols
    else:
        K2, N = b_rows, b_cols
    assert K == K2, (a.shape, b.shape)
    shard = b.shape[2] if b_blocks else None
    tm = _pick(M, tm)
    tn = _pick(shard if (b_blocks and not tb) else (out_blocks or N), tn)
    tk = _pick(shard if (b_blocks and tb) else K, tk)
    nk = K // tk
    dn = (((0 if ta else 1,), (1 if tb else 0,)), ((), ()))
    has_add = add is not None
    assert not (has_add and out_blocks)

    def body(*refs):
        a_ref, b_ref = refs[0], refs[1]
        add_ref = refs[2] if has_add else None
        o_ref = refs[3] if has_add else refs[2]
        bv = b_ref[0] if b_blocks else b_ref[...]
        p = lax.dot_general(a_ref[...].astype(_CD), bv.astype(_CD), dn, preferred_element_type=F32)

        def finish(r):
            if has_add:
                r = r + add_ref[...]
            if out_blocks:
                o_ref[0] = r.astype(out_dtype)
            else:
                o_ref[...] = r.astype(out_dtype)

        if nk == 1:
            finish(p)
        else:
            acc_ref = refs[-1]
            k = pl.program_id(2)

            @pl.when(k == 0)
            def _():
                acc_ref[...] = p

            @pl.when(k > 0)
            def _():
                acc_ref[...] += p

            @pl.when(k == nk - 1)
            def _():
                finish(acc_ref[...])

    a_spec = pl.BlockSpec((tk, tm), lambda i, j, k: (k, i)) if ta else pl.BlockSpec((tm, tk), lambda i, j, k: (i, k))
    if b_blocks and tb:
        per = shard // tk
        b_spec = pl.BlockSpec((1, tn, tk), lambda i, j, k: (k // per, j, k % per))
    elif b_blocks:
        per = shard // tn
        b_spec = pl.BlockSpec((1, tk, tn), lambda i, j, k: (j // per, k, j % per))
    else:
        b_spec = pl.BlockSpec((tn, tk), lambda i, j, k: (j, k)) if tb else pl.BlockSpec((tk, tn), lambda i, j, k: (k, j))
    if out_blocks:
        oper = out_blocks // tn
        o_spec = pl.BlockSpec((1, tm, tn), lambda i, j, k: (j // oper, i, j % oper))
        out_shape = jax.ShapeDtypeStruct((N // out_blocks, M, out_blocks), out_dtype)
    else:
        o_spec = pl.BlockSpec((tm, tn), lambda i, j, k: (i, j))
        out_shape = jax.ShapeDtypeStruct((M, N), out_dtype)
    in_specs = [a_spec, b_spec] + ([o_spec] if has_add else [])
    args = (a, b) + ((add,) if has_add else ())
    return _pcall(
        body, after, name=name, grid=(M // tm, N // tn, nk), in_specs=in_specs, out_specs=o_spec,
        out_shape=out_shape,
        scratch_shapes=[pltpu.VMEM((tm, tn), F32)] if nk > 1 else [],
        compiler_params=_params("parallel", "parallel", "arbitrary"),
    )(*args)


def _rms_fwd(x, g, *, name, tm=512, after=None):
    S, D = x.shape

    def body(x_ref, g_ref, h_ref):
        xv = x_ref[...]
        r = lax.rsqrt(jnp.mean(xv * xv, axis=-1, keepdims=True) + RMS_EPS)
        h_ref[...] = ((xv * r) * g_ref[...]).astype(h_ref.dtype)

    row = pl.BlockSpec((tm, D), lambda i: (i, 0))
    return _pcall(body, after, name=name, grid=(S // tm,), in_specs=[row, pl.BlockSpec((1, D), lambda i: (0, 0))],
                  out_specs=row, out_shape=jax.ShapeDtypeStruct((S, D), _CD), compiler_params=_params("parallel"))(x, g)


def _rms_bwd(x, g, dh, dres, *, name, tm=512, after=None):
    S, D = x.shape

    def body(x_ref, g_ref, dh_ref, dres_ref, dx_ref, dg_ref):
        xv = x_ref[...]
        r = lax.rsqrt(jnp.mean(xv * xv, axis=-1, keepdims=True) + RMS_EPS)
        xh = xv * r
        dhv = dh_ref[...]
        dxh = dhv * g_ref[...]
        dx_ref[...] = dres_ref[...] + r * (dxh - xh * jnp.mean(dxh * xh, axis=-1, keepdims=True))
        part = jnp.sum(dhv * xh, axis=0, keepdims=True)

        @pl.when(pl.program_id(0) == 0)
        def _():
            dg_ref[...] = part

        @pl.when(pl.program_id(0) > 0)
        def _():
            dg_ref[...] += part

    row = pl.BlockSpec((tm, D), lambda i: (i, 0))
    vec = pl.BlockSpec((1, D), lambda i: (0, 0))
    return _pcall(body, after, name=name, grid=(S // tm,), in_specs=[row, vec, row, row], out_specs=[row, vec],
                  out_shape=[jax.ShapeDtypeStruct((S, D), F32), jax.ShapeDtypeStruct((1, D), F32)],
                  compiler_params=_params("arbitrary"))(x, g, dh, dres)


def _loss_head(x, g, tgt, *, name, tm=512):
    S, D = x.shape

    def body(x_ref, g_ref, t_ref, loss_ref, dx_ref, dg_ref):
        xv = x_ref[...]
        gv = g_ref[...]
        r = lax.rsqrt(jnp.mean(xv * xv, axis=-1, keepdims=True) + RMS_EPS)
        xh = xv * r
        err = xh * gv - t_ref[...]
        lpart = 0.5 * jnp.sum(jnp.mean(err * err, axis=-1, keepdims=True), axis=0, keepdims=True)
        dy = err * (1.0 / D)
        dxh = dy * gv
        dx_ref[...] = r * (dxh - xh * jnp.mean(dxh * xh, axis=-1, keepdims=True))
        gpart = jnp.sum(dy * xh, axis=0, keepdims=True)

        @pl.when(pl.program_id(0) == 0)
        def _():
            loss_ref[...] = lpart
            dg_ref[...] = gpart

        @pl.when(pl.program_id(0) > 0)
        def _():
            loss_ref[...] += lpart
            dg_ref[...] += gpart

    row = pl.BlockSpec((tm, D), lambda i: (i, 0))
    vec = pl.BlockSpec((1, D), lambda i: (0, 0))
    one = pl.BlockSpec((1, 1), lambda i: (0, 0))
    return _pcall(body, name=name, grid=(S // tm,), in_specs=[row, vec, row], out_specs=[one, row, vec],
                  out_shape=[jax.ShapeDtypeStruct((1, 1), F32), jax.ShapeDtypeStruct((S, D), F32),
                             jax.ShapeDtypeStruct((1, D), F32)],
                  compiler_params=_params("arbitrary"))(x, g, tgt)


def _sigmoid(z):
    return 1.0 / (1.0 + jnp.exp(-z))


def _gate_fwd(gl, bg, ya, yb, *, name, tm=512):
    S, D = ya.shape

    def body(za_ref, zb_ref, ba_ref, bb_ref, ya_ref, yb_ref, o_ref):
        ga = _sigmoid(za_ref[...] + ba_ref[...])
        gb = _sigmoid(zb_ref[...] + bb_ref[...])
        o_ref[...] = (ga * ya_ref[...] + gb * yb_ref[...]).astype(o_ref.dtype)

    lo = pl.BlockSpec((tm, D), lambda i: (i, 0))
    hi = pl.BlockSpec((tm, D), lambda i: (i, 1))
    vlo = pl.BlockSpec((1, D), lambda i: (0, 0))
    vhi = pl.BlockSpec((1, D), lambda i: (0, 1))
    return _pcall(body, name=name, grid=(S // tm,), in_specs=[lo, hi, vlo, vhi, lo, lo], out_specs=lo,
                  out_shape=jax.ShapeDtypeStruct((S, D), _CD), compiler_params=_params("parallel"))(gl, gl, bg, bg, ya, yb)


def _gate_bwd(dm, gl, bg, ya, yb, *, name, tm=256):
    S, D = ya.shape

    def body(dm_ref, za_ref, zb_ref, ba_ref, bb_ref, ya_ref, yb_ref, dya_ref, dyb_ref, dgl_ref, dbg_ref):
        dmv = dm_ref[...]
        ga = _sigmoid(za_ref[...] + ba_ref[...])
        gb = _sigmoid(zb_ref[...] + bb_ref[...])
        dya_ref[...] = (dmv * ga).astype(dya_ref.dtype)
        dyb_ref[...] = (dmv * gb).astype(dyb_ref.dtype)
        dza = dmv * ya_ref[...] * ga * (1.0 - ga)
        dzb = dmv * yb_ref[...] * gb * (1.0 - gb)
        dgl_ref[:, :D] = dza.astype(dgl_ref.dtype)
        dgl_ref[:, D:] = dzb.astype(dgl_ref.dtype)
        pa = jnp.sum(dza, axis=0, keepdims=True)
        pb = jnp.sum(dzb, axis=0, keepdims=True)

        @pl.when(pl.program_id(0) == 0)
        def _():
            dbg_ref[:, :D] = pa
            dbg_ref[:, D:] = pb

        @pl.when(pl.program_id(0) > 0)
        def _():
            dbg_ref[:, :D] += pa
            dbg_ref[:, D:] += pb

    lo = pl.BlockSpec((tm, D), lambda i: (i, 0))
    hi = pl.BlockSpec((tm, D), lambda i: (i, 1))
    vlo = pl.BlockSpec((1, D), lambda i: (0, 0))
    vhi = pl.BlockSpec((1, D), lambda i: (0, 1))
    wide = pl.BlockSpec((tm, 2 * D), lambda i: (i, 0))
    vwide = pl.BlockSpec((1, 2 * D), lambda i: (0, 0))
    return _pcall(body, name=name, grid=(S // tm,), in_specs=[lo, lo, hi, vlo, vhi, lo, lo],
                  out_specs=[lo, lo, wide, vwide],
                  out_shape=[jax.ShapeDtypeStruct((S, D), _CD), jax.ShapeDtypeStruct((S, D), _CD),
                             jax.ShapeDtypeStruct((S, 2 * D), _CD), jax.ShapeDtypeStruct((1, 2 * D), F32)],
                  compiler_params=_params("arbitrary"))(dm, gl, gl, bg, bg, ya, yb)


def _swiglu_fwd(gu, *, name, tm=256):
    S, F2 = gu.shape
    F = F2 // 2

    def body(g_ref, u_ref, o_ref):
        gv = g_ref[...].astype(F32)
        o_ref[...] = (gv * _sigmoid(gv) * u_ref[...].astype(F32)).astype(o_ref.dtype)

    lo = pl.BlockSpec((tm, F), lambda i: (i, 0))
    hi = pl.BlockSpec((tm, F), lambda i: (i, 1))
    return _pcall(body, name=name, grid=(S // tm,), in_specs=[lo, hi], out_specs=lo,
                  out_shape=jax.ShapeDtypeStruct((S, F), _CD), compiler_params=_params("parallel"))(gu, gu)


def _swiglu_bwd(dact, gu, *, name, tm=256):
    S, F2 = gu.shape
    F = F2 // 2

    def body(d_ref, g_ref, u_ref, o_ref):
        dv = d_ref[...].astype(F32)
        gv = g_ref[...].astype(F32)
        sg = _sigmoid(gv)
        o_ref[:, :F] = (dv * u_ref[...].astype(F32) * (sg * (1.0 + gv * (1.0 - sg)))).astype(o_ref.dtype)
        o_ref[:, F:] = (dv * (gv * sg)).astype(o_ref.dtype)

    lo = pl.BlockSpec((tm, F), lambda i: (i, 0))
    hi = pl.BlockSpec((tm, F), lambda i: (i, 1))
    return _pcall(body, name=name, grid=(S // tm,), in_specs=[lo, lo, hi],
                  out_specs=pl.BlockSpec((tm, F2), lambda i: (i, 0)),
                  out_shape=jax.ShapeDtypeStruct((S, F2), _CD), compiler_params=_params("parallel"))(dact, gu, gu)


def _split3(x):
    hi = x.astype(jnp.bfloat16)
    r1 = x - hi.astype(F32)
    mid = r1.astype(jnp.bfloat16)
    lo = (r1 - mid.astype(F32)).astype(jnp.bfloat16)
    return hi, mid, lo


def _ones_dot_left(ones, x):
    return sum(jnp.dot(ones, p, preferred_element_type=F32) for p in _split3(x))


def _ones_dot_right(x, ones):
    return sum(jnp.dot(p, ones, preferred_element_type=F32) for p in _split3(x))


def _head_sum(x):
    n = x.shape[1]
    r = lax.broadcasted_iota(jnp.int32, (n, n), 0) // HEAD_DIM
    c = lax.broadcasted_iota(jnp.int32, (n, n), 1) // HEAD_DIM
    return _ones_dot_right(x, (r == c).astype(jnp.bfloat16))


def _log_sigmoid(z):
    e = jnp.exp(-jnp.abs(z))
    t = 1.0 + e
    log1p_e = jnp.where(t == 1.0, e, jnp.log(t) * (e / jnp.where(t == 1.0, 1.0, t - 1.0)))
    return jnp.minimum(z, 0.0) - log1p_e


def _fox_cumsum(zf, bf, *, name):
    S, W = zf.shape
    nb = S // 128

    def body(z_ref, b_ref, c_ref):
        tri = (lax.broadcasted_iota(jnp.int32, (128, 128), 0) >= lax.broadcasted_iota(jnp.int32, (128, 128), 1))
        tri = tri.astype(jnp.bfloat16)

        def step(i, carry):
            rows = pl.ds(pl.multiple_of(i * 128, 128), 128)
            lf = _log_sigmoid(z_ref[rows, :] + b_ref[...])
            cb = _ones_dot_left(tri, lf) + carry
            c_ref[rows, :] = cb
            return cb[127:128, :]

        lax.fori_loop(0, nb, step, jnp.zeros((1, W), F32))

    return _pcall(body, name=name, out_shape=jax.ShapeDtypeStruct((S, W), F32),
                  compiler_params=pltpu.CompilerParams(vmem_limit_bytes=VMEM_LIMIT))(zf, bf)


def _fox_cumsum_bwd(dc, zf, bf, *, name):
    S, W = zf.shape
    nb = S // 128

    def body(dc_ref, z_ref, b_ref, dz_ref, db_ref):
        tri = (lax.broadcasted_iota(jnp.int32, (128, 128), 0) <= lax.broadcasted_iota(jnp.int32, (128, 128), 1))
        tri = tri.astype(jnp.bfloat16)

        def step(k, carry):
            tail, acc = carry
            i = nb - 1 - k
            rows = pl.ds(pl.multiple_of(i * 128, 128), 128)
            dlf = _ones_dot_left(tri, dc_ref[rows, :]) + tail
            dz = dlf * _sigmoid(-(z_ref[rows, :] + b_ref[...]))
            dz_ref[rows, :] = dz
            return dlf[0:1, :], acc + jnp.sum(dz, axis=0, keepdims=True)

        _, acc = lax.fori_loop(0, nb, step, (jnp.zeros((1, W), F32), jnp.zeros((1, W), F32)))
        db_ref[...] = acc

    return _pcall(body, name=name,
                  out_shape=[jax.ShapeDtypeStruct((S, W), F32), jax.ShapeDtypeStruct((1, W), F32)],
                  compiler_params=pltpu.CompilerParams(vmem_limit_bytes=VMEM_LIMIT))(dc, zf, bf)


def _dil_slopes(group):
    h = np.arange(1, N_DIL_GROUPS * DIL_HEADS + 1, dtype=np.float32)
    s = (np.float32(2.0) ** (np.float32(-8.0) * h / np.float32(N_DIL_GROUPS * DIL_HEADS))).astype(np.float32)
    return [float(v) for v in s.reshape(N_DIL_GROUPS, DIL_HEADS)[group]]


def _dil_tiles(i, n, blocks_per_seq):
    qi = lax.broadcasted_iota(jnp.int32, (DIL_W, DIL_W), 0)
    kj = lax.broadcasted_iota(jnp.int32, (DIL_W, DIL_W), 1)
    first = ((4 * n + i) % blocks_per_seq) == 0
    valid_prev = jnp.logical_and(kj >= qi, jnp.logical_not(first))
    valid_cur = kj <= qi
    rel_prev = (qi - kj + DIL_W).astype(F32)
    rel_cur = (qi - kj).astype(F32)
    return valid_prev, valid_cur, rel_prev, rel_cur


CHUNK = 4 * DIL_W


def _dil_fwd(q, k, v, group, *, name):
    S = q.shape[0]
    dilation = DIL_PAIRS[group][1]
    bps = (S // dilation) // DIL_W
    slopes = _dil_slopes(group)
    nt = (((1,), (1,)), ((), ()))

    def body(q_ref, k_ref, v_ref, kp_ref, vp_ref, o_ref, l_ref):
        n = pl.program_id(0)
        for i in range(4):
            valid_prev, valid_cur, rel_prev, rel_cur = _dil_tiles(i, n, bps)
            rows = slice(i * DIL_W, (i + 1) * DIL_W)
            prow = slice((i - 1) * DIL_W, i * DIL_W)
            for h in range(DIL_HEADS):
                cols = slice(h * HEAD_DIM, (h + 1) * HEAD_DIM)
                qh = q_ref[rows, cols]
                kc, vc = k_ref[rows, cols], v_ref[rows, cols]
                kp = kp_ref[:, cols] if i == 0 else k_ref[prow, cols]
                vp = vp_ref[:, cols] if i == 0 else v_ref[prow, cols]
                sl = slopes[h] * dilation
                sp = lax.dot_general(qh, kp, nt, preferred_element_type=F32) * ATTN_SCALE - sl * rel_prev
                sc = lax.dot_general(qh, kc, nt, preferred_element_type=F32) * ATTN_SCALE - sl * rel_cur
                sp = jnp.where(valid_prev, sp, NEG_INF)
                sc = jnp.where(valid_cur, sc, NEG_INF)
                m = jnp.maximum(jnp.max(sp, axis=-1, keepdims=True), jnp.max(sc, axis=-1, keepdims=True))
                pp, pc = jnp.exp(sp - m), jnp.exp(sc - m)
                den = jnp.sum(pp, axis=-1, keepdims=True) + jnp.sum(pc, axis=-1, keepdims=True)
                acc = (jnp.dot(pp.astype(_CD), vp, preferred_element_type=F32)
                       + jnp.dot(pc.astype(_CD), vc, preferred_element_type=F32))
                o_ref[rows, cols] = acc / den
                l_ref[rows, cols] = jnp.broadcast_to(m + jnp.log(den), (DIL_W, HEAD_DIM))

    cur = pl.BlockSpec((CHUNK, DIL_OUT), lambda n: (n, 0))
    prev = pl.BlockSpec((DIL_W, DIL_OUT), lambda n: (jnp.maximum(4 * n - 1, 0), 0))
    return _pcall(body, name=name, grid=(S // CHUNK,), in_specs=[cur, cur, cur, prev, prev], out_specs=[cur, cur],
                  out_shape=[jax.ShapeDtypeStruct((S, DIL_OUT), F32), jax.ShapeDtypeStruct((S, DIL_OUT), F32)],
                  compiler_params=_params("parallel"))(q, k, v, k, v)


def _dil_bwd(q, k, v, o, lse, do, dlse, group, *, name):
    S = q.shape[0]
    dilation = DIL_PAIRS[group][1]
    bps = (S // dilation) // DIL_W
    slopes = _dil_slopes(group)
    nchunk = S // CHUNK
    nt = (((1,), (1,)), ((), ()))
    tn = (((0,), (0,)), ((), ()))

    def body(q_ref, k_ref, v_ref, kp_ref, vp_ref, o_ref, l_ref, do_ref, dl_ref, dq_ref, dk_ref, dv_ref,
             dk_s, dv_s):
        step = pl.program_id(0)
        n = nchunk - 1 - step

        @pl.when(step == 0)
        def _():
            dk_s[CHUNK:, :] = jnp.zeros((DIL_W, DIL_OUT), F32)
            dv_s[CHUNK:, :] = jnp.zeros((DIL_W, DIL_OUT), F32)

        dk_s[:CHUNK, :] = jnp.zeros((CHUNK, DIL_OUT), F32)
        dv_s[:CHUNK, :] = jnp.zeros((CHUNK, DIL_OUT), F32)
        for i in range(4):
            valid_prev, valid_cur, rel_prev, rel_cur = _dil_tiles(i, n, bps)
            rows = slice(i * DIL_W, (i + 1) * DIL_W)
            prow = slice((i - 1) * DIL_W, i * DIL_W)
            s_prev = slice(i * DIL_W, (i + 1) * DIL_W)
            s_cur = slice((i + 1) * DIL_W, (i + 2) * DIL_W)
            for h in range(DIL_HEADS):
                cols = slice(h * HEAD_DIM, (h + 1) * HEAD_DIM)
                qh = q_ref[rows, cols]
                kc, vc = k_ref[rows, cols], v_ref[rows, cols]
                kp = kp_ref[:, cols] if i == 0 else k_ref[prow, cols]
                vp = vp_ref[:, cols] if i == 0 else v_ref[prow, cols]
                sl = slopes[h] * dilation
                lh = l_ref[rows, h * HEAD_DIM:h * HEAD_DIM + 1]
                sp = lax.dot_general(qh, kp, nt, preferred_element_type=F32) * ATTN_SCALE - sl * rel_prev
                sc = lax.dot_general(qh, kc, nt, preferred_element_type=F32) * ATTN_SCALE - sl * rel_cur
                pp = jnp.exp(jnp.where(valid_prev, sp, NEG_INF) - lh)
                pc = jnp.exp(jnp.where(valid_cur, sc, NEG_INF) - lh)
                doh = do_ref[rows, cols]
                dsum = jnp.sum(doh * o_ref[rows, cols], axis=-1, keepdims=True)
                shift = dl_ref[rows, h * HEAD_DIM:h * HEAD_DIM + 1] - dsum
                dob = doh.astype(_CD)
                dsp = pp * (lax.dot_general(dob, vp, nt, preferred_element_type=F32) + shift)
                dsc = pc * (lax.dot_general(dob, vc, nt, preferred_element_type=F32) + shift)
                dspb = (dsp * ATTN_SCALE).astype(_CD)
                dscb = (dsc * ATTN_SCALE).astype(_CD)
                dq_ref[rows, cols] = (jnp.dot(dspb, kp, preferred_element_type=F32)
                                      + jnp.dot(dscb, kc, preferred_element_type=F32)).astype(dq_ref.dtype)
                dk_s[s_prev, cols] += lax.dot_general(dspb, qh, tn, preferred_element_type=F32)
                dk_s[s_cur, cols] += lax.dot_general(dscb, qh, tn, preferred_element_type=F32)
                dv_s[s_prev, cols] += lax.dot_general(pp.astype(_CD), dob, tn, preferred_element_type=F32)
                dv_s[s_cur, cols] += lax.dot_general(pc.astype(_CD), dob, tn, preferred_element_type=F32)
        dk_ref[...] = dk_s[DIL_W:, :].astype(dk_ref.dtype)
        dv_ref[...] = dv_s[DIL_W:, :].astype(dv_ref.dtype)
        dk_s[CHUNK:, :] = dk_s[:DIL_W, :]
        dv_s[CHUNK:, :] = dv_s[:DIL_W, :]

    cur = pl.BlockSpec((CHUNK, DIL_OUT), lambda s: (nchunk - 1 - s, 0))
    prev = pl.BlockSpec((DIL_W, DIL_OUT), lambda s: (jnp.maximum(4 * (nchunk - 1 - s) - 1, 0), 0))
    shp = jax.ShapeDtypeStruct((S, DIL_OUT), _CD)
    return _pcall(body, name=name, grid=(nchunk,), in_specs=[cur, cur, cur, prev, prev, cur, cur, cur, cur],
                  out_specs=[cur, cur, cur], out_shape=[shp, shp, shp],
                  scratch_shapes=[pltpu.VMEM((CHUNK + DIL_W, DIL_OUT), F32), pltpu.VMEM((CHUNK + DIL_W, DIL_OUT), F32)],
                  compiler_params=_params("arbitrary"))(q, k, v, k, v, o, lse, do, dlse)


def _dil_mix_fwd(os_, ls_, *, name, tm=512):
    S, W = os_[0].shape

    def body(o0, o1, o2, l0, l1, l2, out_ref):
        ls = [l0[...], l1[...], l2[...]]
        m = jnp.maximum(jnp.maximum(ls[0], ls[1]), ls[2])
        es = [jnp.exp(l - m) for l in ls]
        den = es[0] + es[1] + es[2]
        out_ref[...] = ((es[0] * o0[...] + es[1] * o1[...] + es[2] * o2[...]) / den).astype(out_ref.dtype)

    row = pl.BlockSpec((tm, W), lambda i: (i, 0))
    return _pcall(body, name=name, grid=(S // tm,), in_specs=[row] * 6, out_specs=row,
                  out_shape=jax.ShapeDtypeStruct((S, W), _CD), compiler_params=_params("parallel"))(*os_, *ls_)


def _dil_mix_bwd(doa, os_, ls_, *, name, tm=512, after=None):
    S, W = doa.shape

    def body(d_ref, o0, o1, o2, l0, l1, l2, do0, do1, do2, dl0, dl1, dl2):
        dv = d_ref[...]
        ls = [l0[...], l1[...], l2[...]]
        m = jnp.maximum(jnp.maximum(ls[0], ls[1]), ls[2])
        es = [jnp.exp(l - m) for l in ls]
        den = es[0] + es[1] + es[2]
        al = [e / den for e in es]
        da = [_head_sum(dv * o[...]) for o in (o0, o1, o2)]
        mean = al[0] * da[0] + al[1] * da[1] + al[2] * da[2]
        for a, d_, do_ref, dl_ref in zip(al, da, (do0, do1, do2), (dl0, dl1, dl2)):
            do_ref[...] = a * dv
            dl_ref[...] = a * (d_ - mean)

    row = pl.BlockSpec((tm, W), lambda i: (i, 0))
    shp = jax.ShapeDtypeStruct((S, W), F32)
    return _pcall(body, after, name=name, grid=(S // tm,), in_specs=[row] * 7, out_specs=[row] * 6, out_shape=[shp] * 6,
                  compiler_params=_params("parallel"))(doa, *os_, *ls_)


FOX_T = 512


PACK = 2 * HEAD_DIM
HEAD_PAIRS = N_FOX_HEADS // 2
Q_BLOCK0 = (3 * DIL_WIDTH) // PACK
K_BLOCK0 = (3 * DIL_WIDTH + FOX_WIDTH) // PACK
V_BLOCK0 = (3 * DIL_WIDTH + 2 * FOX_WIDTH) // PACK


def _pieces(x):
    hi = x.astype(jnp.bfloat16).astype(F32)
    r = x - hi
    mid = r.astype(jnp.bfloat16).astype(F32)
    lo = (r - mid).astype(jnp.bfloat16).astype(F32)
    return [hi, mid, lo]


def _extras(first, second, rows):
    lane = lax.broadcasted_iota(jnp.int32, (rows, HEAD_DIM), 1)
    out = jnp.zeros((rows, HEAD_DIM), F32)
    for idx, val in enumerate(list(first) + list(second)):
        out = jnp.where(lane == idx, val, out)
    return out


def _head_column(c, h):
    lane = lax.broadcasted_iota(jnp.int32, c.shape, 1)
    return jnp.sum(jnp.where(lane == h, c, 0.0), axis=1, keepdims=True)


ONES3 = [1.0, 1.0, 1.0]
ZEROS3 = [0.0, 0.0, 0.0]


def _fox_pack_fwd(qkv, c, *, name, tm=512):
    S = qkv.shape[0]

    def body(q_ref, k_ref, v_ref, c_ref, qo_ref, ko_ref, vo_ref):
        hp = pl.program_id(1)
        cv = c_ref[...]
        for hh in range(2):
            ch = _pieces(_head_column(cv, 2 * hp + hh))
            src = slice(hh * HEAD_DIM, (hh + 1) * HEAD_DIM)
            lo = slice(hh * PACK, hh * PACK + HEAD_DIM)
            hi = slice(hh * PACK + HEAD_DIM, (hh + 1) * PACK)
            qo_ref[:, lo] = (q_ref[:, src].astype(F32) * ATTN_SCALE).astype(qo_ref.dtype)
            qo_ref[:, hi] = _extras(ch, ONES3, tm).astype(qo_ref.dtype)
            ko_ref[:, lo] = k_ref[:, src]
            ko_ref[:, hi] = _extras(ONES3, [-p for p in ch], tm).astype(ko_ref.dtype)
            vo_ref[:, lo] = v_ref[:, src]
            vo_ref[:, hi] = _extras(ONES3, ZEROS3, tm).astype(vo_ref.dtype)

    def src(block0):
        return pl.BlockSpec((tm, PACK), lambda i, hp: (i, block0 + hp))

    out = pl.BlockSpec((tm, 2 * PACK), lambda i, hp: (i, hp))
    shp = jax.ShapeDtypeStruct((S, N_FOX_HEADS * PACK), _CD)
    return _pcall(body, name=name, grid=(S // tm, HEAD_PAIRS),
                  in_specs=[src(Q_BLOCK0), src(K_BLOCK0), src(V_BLOCK0), pl.BlockSpec((tm, PACK), lambda i, hp: (i, 0))],
                  out_specs=[out, out, out], out_shape=[shp, shp, shp],
                  compiler_params=_params("parallel", "parallel"))(qkv, qkv, qkv, c)


def _fox_fwd(qp, kp, vp, *, name):
    S = qp.shape[0]
    nt = S // FOX_T
    nt_dims = (((1,), (1,)), ((), ()))
    tn_dims = (((0,), (0,)), ((), ()))

    def body(i_tab, j_tab, q_ref, k_ref, v_ref, o_ref, l_ref, m_s, acc_s):
        t = pl.program_id(1)
        i, j = i_tab[t], j_tab[t]

        @pl.when(j == 0)
        def _():
            m_s[...] = jnp.full((2, 1, FOX_T), NEG_INF, F32)
            acc_s[...] = jnp.zeros((2, PACK, FOX_T), F32)

        def tile(diagonal):
            for hh in range(2):
                cols = slice(hh * PACK, (hh + 1) * PACK)
                st = lax.dot_general(k_ref[:, cols], q_ref[:, cols], nt_dims, preferred_element_type=F32)
                if diagonal:
                    key = lax.broadcasted_iota(jnp.int32, (FOX_T, FOX_T), 0)
                    qry = lax.broadcasted_iota(jnp.int32, (FOX_T, FOX_T), 1)
                    st = jnp.where(key <= qry, st, NEG_INF)
                m_old = m_s[hh]
                m_new = jnp.maximum(m_old, jnp.max(st, axis=0, keepdims=True))
                pt = jnp.exp(st - m_new)
                acc_s[hh] = jnp.exp(m_old - m_new) * acc_s[hh] + lax.dot_general(
                    v_ref[:, cols], pt.astype(_CD), tn_dims, preferred_element_type=F32)
                m_s[hh] = m_new

        @pl.when(j < i)
        def _():
            tile(False)

        @pl.when(j == i)
        def _():
            tile(True)
            for hh in range(2):
                acc = acc_s[hh]
                den = acc[HEAD_DIM:HEAD_DIM + 1, :]
                cols = slice(hh * HEAD_DIM, (hh + 1) * HEAD_DIM)
                o_ref[:, cols] = (acc[:HEAD_DIM, :] / den).T
                l_ref[:, cols] = jnp.broadcast_to(m_s[hh] + jnp.log(den), (HEAD_DIM, FOX_T)).T

    pairs = [(i, j) for i in range(nt) for j in range(i + 1)]
    i_tab = jnp.asarray([p[0] for p in pairs], jnp.int32)
    j_tab = jnp.asarray([p[1] for p in pairs], jnp.int32)
    qs = pl.BlockSpec((FOX_T, 2 * PACK), lambda hp, t, it, jt: (it[t], hp))
    ks = pl.BlockSpec((FOX_T, 2 * PACK), lambda hp, t, it, jt: (jt[t], hp))
    os_ = pl.BlockSpec((FOX_T, PACK), lambda hp, t, it, jt: (it[t], hp))
    shp = jax.ShapeDtypeStruct((S, FOX_WIDTH), F32)
    grid_spec = pltpu.PrefetchScalarGridSpec(
        num_scalar_prefetch=2, grid=(HEAD_PAIRS, len(pairs)), in_specs=[qs, ks, ks], out_specs=[os_, os_],
        scratch_shapes=[pltpu.VMEM((2, 1, FOX_T), F32), pltpu.VMEM((2, PACK, FOX_T), F32)])
    return _pcall(body, name=name, grid_spec=grid_spec, out_shape=[shp, shp],
                  compiler_params=_params("parallel", "arbitrary"))(i_tab, j_tab, qp, kp, vp)


def _fox_pack_bwd(qkv, c, o, lse, do, *, name, tm=512, after=None):
    S = qkv.shape[0]

    def body(q_ref, c_ref, o_ref, l_ref, do_ref, qo_ref, do_out_ref):
        hp = pl.program_id(1)
        cv = c_ref[...]
        for hh in range(2):
            src = slice(hh * HEAD_DIM, (hh + 1) * HEAD_DIM)
            lo = slice(hh * PACK, hh * PACK + HEAD_DIM)
            hi = slice(hh * PACK + HEAD_DIM, (hh + 1) * PACK)
            shift = _head_column(cv, 2 * hp + hh) - l_ref[:, hh * HEAD_DIM:hh * HEAD_DIM + 1]
            dov = do_ref[:, src]
            dsum = jnp.sum(dov * o_ref[:, src], axis=-1, keepdims=True)
            qo_ref[:, lo] = (q_ref[:, src].astype(F32) * ATTN_SCALE).astype(qo_ref.dtype)
            qo_ref[:, hi] = _extras(_pieces(shift), ONES3, tm).astype(qo_ref.dtype)
            do_out_ref[:, lo] = dov.astype(do_out_ref.dtype)
            do_out_ref[:, hi] = _extras(_pieces(-dsum), ZEROS3, tm).astype(do_out_ref.dtype)

    pair = pl.BlockSpec((tm, PACK), lambda i, hp: (i, hp))
    out = pl.BlockSpec((tm, 2 * PACK), lambda i, hp: (i, hp))
    shp = jax.ShapeDtypeStruct((S, N_FOX_HEADS * PACK), _CD)
    return _pcall(body, after, name=name, grid=(S // tm, HEAD_PAIRS),
                  in_specs=[pl.BlockSpec((tm, PACK), lambda i, hp: (i, Q_BLOCK0 + hp)),
                            pl.BlockSpec((tm, PACK), lambda i, hp: (i, 0)), pair, pair, pair],
                  out_specs=[out, out], out_shape=[shp, shp],
                  compiler_params=_params("parallel", "parallel"))(qkv, c, o, lse, do)


def _fox_bwd(qp, kp, vp, dop, *, name):
    S = qp.shape[0]
    nt = S // FOX_T
    nt_dims = (((1,), (1,)), ((), ()))
    tn_dims = (((0,), (0,)), ((), ()))

    def body(i_tab, j_tab, q_ref, k_ref, v_ref, do_ref, dq_ref, dk_ref, dv_ref, dc_ref, dr_ref,
             dq_s, dk_s, dv_s, dc_s, dr_s):
        t = pl.program_id(1)
        i, j = i_tab[t], j_tab[t]

        @pl.when(t == 0)
        def _():
            dq_s[...] = jnp.zeros((S, 2 * PACK), F32)
            dr_s[...] = jnp.zeros((2, 1, S), F32)

        @pl.when(i == j)
        def _():
            dk_s[...] = jnp.zeros((FOX_T, 2 * PACK), F32)
            dv_s[...] = jnp.zeros((FOX_T, 2 * PACK), F32)
            dc_s[...] = jnp.zeros((2, FOX_T, 1), F32)

        def tile(diagonal):
            rows = pl.ds(pl.multiple_of(i * FOX_T, FOX_T), FOX_T)
            for hh in range(2):
                cols = slice(hh * PACK, (hh + 1) * PACK)
                qv, kv, vv, dov = q_ref[:, cols], k_ref[:, cols], v_ref[:, cols], do_ref[:, cols]
                pt = jnp.exp(lax.dot_general(kv, qv, nt_dims, preferred_element_type=F32))
                if diagonal:
                    key = lax.broadcasted_iota(jnp.int32, (FOX_T, FOX_T), 0)
                    qry = lax.broadcasted_iota(jnp.int32, (FOX_T, FOX_T), 1)
                    pt = jnp.where(key <= qry, pt, 0.0)
                dst = pt * lax.dot_general(vv, dov, nt_dims, preferred_element_type=F32)
                dsb = dst.astype(_CD)
                dc_s[hh] += jnp.sum(dst, axis=1, keepdims=True)
                dr_s[hh, :, rows] += jnp.sum(dst, axis=0, keepdims=True)
                dv_s[:, cols] += jnp.dot(pt.astype(_CD), dov, preferred_element_type=F32)
                dk_s[:, cols] += jnp.dot(dsb, qv, preferred_element_type=F32)
                dq_s[rows, cols] += lax.dot_general(dsb, kv, tn_dims, preferred_element_type=F32)

        @pl.when(i > j)
        def _():
            tile(False)

        @pl.when(i == j)
        def _():
            tile(True)

        @pl.when(i == nt - 1)
        def _():
            for hh in range(2):
                src = slice(hh * PACK, hh * PACK + HEAD_DIM)
                dst_cols = slice(hh * HEAD_DIM, (hh + 1) * HEAD_DIM)
                dk_ref[:, dst_cols] = dk_s[:, src].astype(dk_ref.dtype)
                dv_ref[:, dst_cols] = dv_s[:, src].astype(dv_ref.dtype)
                dc_ref[:, dst_cols] = jnp.broadcast_to(dc_s[hh], (FOX_T, HEAD_DIM))

        @pl.when(t == len(pairs) - 1)
        def _():
            for hh in range(2):
                dq_ref[:, hh * HEAD_DIM:(hh + 1) * HEAD_DIM] = (
                    dq_s[:, hh * PACK:hh * PACK + HEAD_DIM] * ATTN_SCALE).astype(dq_ref.dtype)
            dr_ref[...] = dr_s[...]

    pairs = [(i, j) for j in range(nt) for i in range(j, nt)]
    i_tab = jnp.asarray([p[0] for p in pairs], jnp.int32)
    j_tab = jnp.asarray([p[1] for p in pairs], jnp.int32)
    qs = pl.BlockSpec((FOX_T, 2 * PACK), lambda hp, t, it, jt: (it[t], hp))
    ks = pl.BlockSpec((FOX_T, 2 * PACK), lambda hp, t, it, jt: (jt[t], hp))
    whole = pl.BlockSpec((S, PACK), lambda hp, t, it, jt: (0, hp))
    cs = pl.BlockSpec((FOX_T, PACK), lambda hp, t, it, jt: (jt[t], hp))
    rs = pl.BlockSpec((2, 1, S), lambda hp, t, it, jt: (hp, 0, 0))
    shp = jax.ShapeDtypeStruct((S, FOX_WIDTH), _CD)
    grid_spec = pltpu.PrefetchScalarGridSpec(
        num_scalar_prefetch=2, grid=(HEAD_PAIRS, len(pairs)), in_specs=[qs, ks, ks, qs],
        out_specs=[whole, cs, cs, cs, rs],
        scratch_shapes=[pltpu.VMEM((S, 2 * PACK), F32), pltpu.VMEM((FOX_T, 2 * PACK), F32),
                        pltpu.VMEM((FOX_T, 2 * PACK), F32), pltpu.VMEM((2, FOX_T, 1), F32),
                        pltpu.VMEM((2, 1, S), F32)])
    return _pcall(body, name=name, grid_spec=grid_spec,
                  out_shape=[shp, shp, shp, jax.ShapeDtypeStruct((S, FOX_WIDTH), F32),
                             jax.ShapeDtypeStruct((N_FOX_HEADS, 1, S), F32)],
                  compiler_params=_params("parallel", "arbitrary"))(i_tab, j_tab, qp, kp, vp, dop)


def _dedilate(t, d):
    if d == 1:
        return t
    S, C = t.shape
    return t.reshape(S // d, d, C).transpose(1, 0, 2).reshape(S, C)


def _redilate(t, d):
    if d == 1:
        return t
    S, C = t.shape
    return t.reshape(d, S // d, C).transpose(1, 0, 2).reshape(S, C)


def _layer_step(x, tgt, w, p, late_weights=None, grad_sink=None, after=None):
    S = x.shape[0]
    h = _rms_fwd(x, p["norm_mix_g"], name="rms_mix")
    qkv = _mm(h, w["qkv"], name="proj_qkv", out_dtype=_CD, tn=768, tm=2048, after=after)
    zf = _mm(h, w["f"], name="proj_f")
    gl = _mm(h, w["g"], name="proj_gate", tn=1024)

    dil_q, dil_k, dil_v = [], [], []
    dil_o, dil_l = [], []
    for g, (_, d) in enumerate(DIL_PAIRS):
        qg = _dedilate(qkv[:, g * DIL_OUT:(g + 1) * DIL_OUT], d)
        kg = _dedilate(qkv[:, DIL_WIDTH + g * DIL_OUT:DIL_WIDTH + (g + 1) * DIL_OUT], d)
        vg = _dedilate(qkv[:, 2 * DIL_WIDTH + g * DIL_OUT:2 * DIL_WIDTH + (g + 1) * DIL_OUT], d)
        og, lg = _dil_fwd(qg, kg, vg, g, name=f"dil_fwd{g}")
        dil_q.append(qg), dil_k.append(kg), dil_v.append(vg)
        dil_o.append(_redilate(og, d)), dil_l.append(_redilate(lg, d))
    o_a = _dil_mix_fwd(dil_o, dil_l, name="dil_mix")

    c = _fox_cumsum(zf, p["b_fgt"], name="fox_cumsum")
    fqp, fkp, fvp = _fox_pack_fwd(qkv, c, name="fox_pack")
    o_b, flse = _fox_fwd(fqp, fkp, fvp, name="fox_fwd")

    if late_weights is not None:
        w = {**w, **late_weights(o_b)}
    y_a = _mm(o_a, w["dil_out"], name="y_a", tn=1024)
    y_b = _mm(o_b, w["fox_out"], name="y_b", tn=1024)
    merged = _gate_fwd(gl, p["b_gate"], y_a, y_b, name="gate_fwd")
    x1 = _mm(merged, w["out"], name="mix_out", add=x)

    h2 = _rms_fwd(x1, p["norm_ffn_g"], name="rms_ffn")
    gu = _mm(h2, w["ffn_in"], name="ffn_in", tn=1408, b_blocks=True, out_dtype=_CD, tm=2048)
    act = _swiglu_fwd(gu, name="swiglu")
    x2 = _mm(act, w["ffn_down"], name="ffn_down", add=x1, tk=2816)

    loss, dx2, dg_final = _loss_head(x2, p["norm_final_g"], tgt, name="loss_head")

    dact = _mm(dx2, w["ffn_down"], name="d_act", tb=True, tn=1408, out_dtype=_CD)
    gw_ffn_down = _mm(act, dx2, name="gw_ffn_down", ta=True, out_dtype=_CD, tm=1408)
    dgu = _swiglu_bwd(dact, gu, name="swiglu_bwd")
    dh2 = _mm(dgu, w["ffn_in"], name="d_h2", tb=True, tk=1408, b_blocks=True, tm=2048)
    gw_ffn_in = _mm(h2, dgu, name="gw_ffn_in", ta=True, out_dtype=_CD, tn=1408, out_blocks=1408)
    sink = grad_sink if grad_sink is not None else (lambda group, grads: None)
    tok = sink("ffn", dict(ffn_in=gw_ffn_in, ffn_down=gw_ffn_down))
    dx1, dg_ffn = _rms_bwd(x1, p["norm_ffn_g"], dh2, dx2, name="rms_ffn_bwd", after=tok)

    dmerged = _mm(dx1, w["out"], name="d_merged", tb=True)
    gw_out = _mm(merged, dx1, name="gw_out", ta=True, out_dtype=_CD)
    dy_a, dy_b, dgl, db_gate = _gate_bwd(dmerged, gl, p["b_gate"], y_a, y_b, name="gate_bwd")
    do_a = _mm(dy_a, w["dil_out"], name="d_o_a", tb=True)
    gw_dil_out = _mm(o_a, dy_a, name="gw_dil_out", ta=True, out_dtype=_CD, tn=1024)
    do_b = _mm(dy_b, w["fox_out"], name="d_o_b", tb=True)
    gw_fox_out = _mm(o_b, dy_b, name="gw_fox_out", ta=True, out_dtype=_CD, tn=1024)
    tok = sink("mix", dict(dil_out=gw_dil_out, fox_out=gw_fox_out, out=gw_out))

    bqp, bdop = _fox_pack_bwd(qkv, c, o_b, flse, do_b, name="fox_pack_bwd", after=tok)
    dqp, dkp, dvp, dck, dcq = _fox_bwd(bqp, fkp, fvp, bdop, name="fox_bwd")
    dc = dcq[:, 0, :].T - dck.reshape(S, N_FOX_HEADS, HEAD_DIM)[:, :, 0]
    dc = jnp.pad(dc, ((0, 0), (0, F_PAD - N_FOX_HEADS)))
    dzf, db_fgt = _fox_cumsum_bwd(dc, zf, p["b_fgt"], name="fox_cumsum_bwd")

    douts = _dil_mix_bwd(do_a, dil_o, dil_l, name="dil_mix_bwd", after=tok)
    dqs, dks, dvs = [], [], []
    for g, (_, d) in enumerate(DIL_PAIRS):
        dq, dk, dv = _dil_bwd(dil_q[g], dil_k[g], dil_v[g], _dedilate(dil_o[g], d), _dedilate(dil_l[g], d),
                              _dedilate(douts[g], d), _dedilate(douts[3 + g], d), g, name=f"dil_bwd{g}")
        dqs.append(_redilate(dq, d)), dks.append(_redilate(dk, d)), dvs.append(_redilate(dv, d))
    dqkv = jnp.concatenate(dqs + dks + dvs + [dqp, dkp, dvp], axis=1)

    gw_qkv = _mm(h, dqkv, name="gw_qkv", ta=True, out_dtype=_CD, tn=768)
    gw_g = _mm(h, dgl, name="gw_gate", ta=True, out_dtype=_CD)
    gw_f = _mm(h, dzf, name="gw_f", ta=True, out_dtype=_CD)
    tok = sink("in", dict(qkv=gw_qkv, f=gw_f, g=gw_g))
    dh = _mm(dqkv, w["qkv"], name="d_h_qkv", tb=True, tk=1920, tm=2048, after=tok)
    dh = _mm(dgl, w["g"], name="d_h_gate", tb=True, add=dh)
    dh = _mm(dzf, w["f"], name="d_h_f", tb=True, add=dh)
    dx, dg_mix = _rms_bwd(x, p["norm_mix_g"], dh, dx1, name="rms_mix_bwd")

    gw = dict(qkv=gw_qkv, f=gw_f, g=gw_g, dil_out=gw_dil_out, fox_out=gw_fox_out, out=gw_out, ffn_in=gw_ffn_in,
              ffn_down=gw_ffn_down)
    small = dict(norm_mix_g=dg_mix, b_fgt=db_fgt, b_gate=db_gate, norm_ffn_g=dg_ffn, norm_final_g=dg_final)
    return loss, dx, gw, small


def _position():
    return lax.axis_index("x"), lax.axis_index("y"), lax.axis_index("c")


def _other_chips(x, y):
    return [(1 - x, y), (x, 1 - y), (1 - x, 1 - y)]


ROW_TILE = 16


def _row_chunks(rows, want=4):
    n = want
    while n > 1 and rows % (n * ROW_TILE):
        n //= 2
    return n


def _gather_weights(shards):
    n = len(shards)
    nq = max(_row_chunks(s.shape[0] // 2) for s in shards)

    def body(*refs):
        ins, outs = refs[:n], refs[n:2 * n]
        send_sems, recv_sems = refs[2 * n:]
        x, y, c = _position()
        mine = 2 * x + y
        chips = _other_chips(x, y)
        started = []

        def pieces(w, core):
            half = ins[w].shape[0] // 2
            size = half // _row_chunks(half)
            return [pl.ds(core * half + q * size, size) for q in range(_row_chunks(half))]

        for w in range(n):
            for r, (cx, cy) in enumerate(chips):
                for q, rows in enumerate(pieces(w, c)):
                    cp = pltpu.make_async_remote_copy(
                        src_ref=ins[w].at[rows, :], dst_ref=outs[w].at[mine, rows, :], send_sem=send_sems.at[w, r, q],
                        recv_sem=recv_sems.at[w, r, q], device_id=(cx, cy, c), device_id_type=MESH)
                    cp.start()
                    started.append(cp)

        def landed(w, r, q, rows, peer):
            cx, cy = chips[r % 3]
            blk = outs[w].at[2 * cx + cy, rows, :]
            return pltpu.make_async_remote_copy(src_ref=blk, dst_ref=blk, send_sem=send_sems.at[w, r, q],
                                                recv_sem=recv_sems.at[w, r, q], device_id=peer, device_id_type=MESH)

        for w in range(n):
            for r, (cx, cy) in enumerate(chips):
                for q, rows in enumerate(pieces(w, c)):
                    landed(w, r, q, rows, (cx, cy, c)).wait_recv()
                    fwd = landed(w, 3 + r, q, rows, (x, y, 1 - c))
                    fwd.start()
                    started.append(fwd)
        for w in range(n):
            for r in range(3):
                for q, rows in enumerate(pieces(w, 1 - c)):
                    landed(w, 3 + r, q, rows, (x, y, 1 - c)).wait_recv()
        for cp in started:
            cp.wait_send()

    return _pcall(
        body, name="gather_weights", in_specs=[HBM_SPEC] * n, out_specs=[HBM_SPEC] * n,
        out_shape=[jax.ShapeDtypeStruct((4,) + s.shape, s.dtype) for s in shards],
        scratch_shapes=[pltpu.SemaphoreType.DMA((n, 6, nq)), pltpu.SemaphoreType.DMA((n, 6, nq))],
    )(*shards)


SEM_SPEC = pl.BlockSpec(memory_space=pltpu.SEMAPHORE)
ANY_SPEC = pl.BlockSpec(memory_space=pl.ANY)
DATAFLOW = pltpu.SideEffectType.DATAFLOW_SIDE_EFFECTING


def _in_hbm(a):
    return pltpu.with_memory_space_constraint(a, pltpu.HBM)


def _split_copy_start(srcs, land_shapes, copies, after, *, name):
    n, m = len(srcs), len(land_shapes)

    def body(*refs):
        src_refs, land_refs = refs[:n], refs[n:n + m]
        send_sems, recv_sems = refs[n + m + 1], refs[n + m + 2]
        token = refs[-1]
        x, y, c = _position()
        for k, (src, dst, peer) in enumerate(copies(x, y, c, src_refs, land_refs)):
            pltpu.make_async_remote_copy(src_ref=src, dst_ref=dst, send_sem=send_sems.at[k], recv_sem=recv_sems.at[k],
                                         device_id=peer, device_id_type=MESH).start()
        token[...] = jnp.zeros_like(token)

    lands = [lax.empty(s.shape, s.dtype) for s in land_shapes]
    count = len(copies(0, 0, 0, srcs, lands))
    out = _pcall(
        body, name=name,
        out_shape=(pltpu.SemaphoreType.DMA((count,)), pltpu.SemaphoreType.DMA((count,)),
                   *[pltpu.HBM(s.shape, s.dtype) for s in srcs], *[pltpu.HBM(s.shape, s.dtype) for s in land_shapes],
                   jax.ShapeDtypeStruct((8, 128), F32)),
        in_specs=[HBM_SPEC] * (n + m) + [ANY_SPEC],
        out_specs=(SEM_SPEC, SEM_SPEC, *[HBM_SPEC] * (n + m), pl.BlockSpec(memory_space=pltpu.VMEM)),
        input_output_aliases={k: 2 + k for k in range(n + m)},
        compiler_params=pltpu.CompilerParams(has_side_effects=DATAFLOW),
    )(*[_in_hbm(s) for s in srcs], *[_in_hbm(l) for l in lands], after)
    return out[0], out[1], list(out[2:2 + n]), list(out[2 + n:2 + n + m]), out[-1]


def _split_copy_wait(send_sems, recv_sems, srcs, lands, copies, after, *, name):
    n, m = len(srcs), len(lands)

    def body(*refs):
        src_refs, land_refs = refs[:n], refs[n:n + m]
        send, recv = refs[n + m], refs[n + m + 1]
        x, y, c = _position()
        for k, (src, dst, peer) in enumerate(copies(x, y, c, src_refs, land_refs)):
            cp = pltpu.make_async_remote_copy(src_ref=src, dst_ref=dst, send_sem=send.at[k], recv_sem=recv.at[k],
                                              device_id=peer, device_id_type=MESH)
            cp.wait_send()
            cp.wait_recv()

    out = _pcall(
        body, name=name,
        out_shape=tuple(pltpu.HBM(s.shape, s.dtype) for s in list(srcs) + list(lands)),
        in_specs=[HBM_SPEC] * (n + m) + [SEM_SPEC, SEM_SPEC, ANY_SPEC], out_specs=tuple([HBM_SPEC] * (n + m)),
        input_output_aliases={k: k for k in range(n + m)},
        compiler_params=pltpu.CompilerParams(has_side_effects=DATAFLOW),
    )(*srcs, *lands, send_sems, recv_sems, after)
    return list(out[:n]), list(out[n:])


def _gather_copies(x, y, c, shard_refs, land_refs):
    out = []
    for s, l in zip(shard_refs, land_refs):
        half = s.shape[0] // 2
        nq = _row_chunks(half)
        for cx, cy in _other_chips(x, y):
            for q in range(nq):
                rows = pl.ds(c * half + q * (half // nq), half // nq)
                out.append((s.at[rows, :], l.at[2 * x + y, rows, :], (cx, cy, c)))
    return out


def _scatter_copies(x, y, c, part_refs, land_refs):
    out = []
    for p, l in zip(part_refs, land_refs):
        nq = _row_chunks(p.shape[1])
        for r, (cx, cy) in enumerate(_other_chips(x, y)):
            for q in range(nq):
                rows = pl.ds(q * (p.shape[1] // nq), p.shape[1] // nq)
                out.append((p.at[2 * cx + cy, rows, :], l.at[r, rows, :], (cx, cy, c)))
    return out


def _forward_halves(lands, *, name):
    n = len(lands)

    def body(*refs):
        ins = refs[:n]
        send_sems, recv_sems = refs[2 * n:]
        x, y, c = _position()
        copies = []
        for w in range(n):
            half = ins[w].shape[1] // 2
            for r, (cx, cy) in enumerate(_other_chips(x, y)):
                blk = ins[w].at[2 * cx + cy, pl.ds(c * half, half), :]
                cp = pltpu.make_async_remote_copy(src_ref=blk, dst_ref=blk, send_sem=send_sems.at[w, r],
                                                  recv_sem=recv_sems.at[w, r], device_id=(x, y, 1 - c),
                                                  device_id_type=MESH)
                cp.start()
                copies.append(cp)
        for w in range(n):
            half = ins[w].shape[1] // 2
            for r, (cx, cy) in enumerate(_other_chips(x, y)):
                blk = ins[w].at[2 * cx + cy, pl.ds((1 - c) * half, half), :]
                pltpu.make_async_remote_copy(src_ref=blk, dst_ref=blk, send_sem=send_sems.at[w, r],
                                             recv_sem=recv_sems.at[w, r], device_id=(x, y, 1 - c),
                                             device_id_type=MESH).wait_recv()
        for cp in copies:
            cp.wait_send()

    return _pcall(
        body, name=name, in_specs=[HBM_SPEC] * n, out_specs=[HBM_SPEC] * n,
        out_shape=[jax.ShapeDtypeStruct(l.shape, l.dtype) for l in lands],
        input_output_aliases={k: k for k in range(n)},
        scratch_shapes=[pltpu.SemaphoreType.DMA((n, 3)), pltpu.SemaphoreType.DMA((n, 3))],
    )(*lands)


def _swap_halves(grads, name="swap_halves"):
    n = len(grads)

    def body(*refs):
        ins, outs = refs[:n], refs[n:2 * n]
        send_sems, recv_sems = refs[2 * n:]
        x, y, c = _position()
        copies = []
        for w in range(n):
            half = ins[w].shape[1] // 2
            cp = pltpu.make_async_remote_copy(
                src_ref=ins[w].at[:, pl.ds((1 - c) * half, half), :], dst_ref=outs[w], send_sem=send_sems.at[w],
                recv_sem=recv_sems.at[w], device_id=(x, y, 1 - c), device_id_type=MESH)
            cp.start()
            copies.append(cp)
        for cp in copies:
            cp.wait()

    return _pcall(
        body, name=name, in_specs=[HBM_SPEC] * n, out_specs=[HBM_SPEC] * n,
        out_shape=[jax.ShapeDtypeStruct((4, g.shape[1] // 2, g.shape[2]), g.dtype) for g in grads],
        scratch_shapes=[pltpu.SemaphoreType.DMA((n,)), pltpu.SemaphoreType.DMA((n,))],
    )(*grads)


def _share_halves(halves):
    n = len(halves)

    def body(*refs):
        ins, outs = refs[:n], refs[n:2 * n]
        send_sems, recv_sems = refs[2 * n:]
        x, y, c = _position()
        copies = []
        for w in range(n):
            cp = pltpu.make_async_remote_copy(src_ref=ins[w], dst_ref=outs[w], send_sem=send_sems.at[w],
                                              recv_sem=recv_sems.at[w], device_id=(x, y, 1 - c), device_id_type=MESH)
            cp.start()
            copies.append(cp)
        for cp in copies:
            cp.wait()

    return _pcall(
        body, name="share_halves", in_specs=[HBM_SPEC] * n, out_specs=[HBM_SPEC] * n,
        out_shape=[jax.ShapeDtypeStruct(h.shape, h.dtype) for h in halves],
        scratch_shapes=[pltpu.SemaphoreType.DMA((n,)), pltpu.SemaphoreType.DMA((n,))],
    )(*halves)


def _sum_small(part):
    rows, width = part.shape

    def body(x_ref, out_ref, all_ref, send_sems, recv_sems):
        x, y, c = _position()
        me, sibling = (x, y, c), (x, y, 1 - c)
        chips = _other_chips(x, y)

        def block(px, py, pc):
            return all_ref.at[pl.ds((4 * px + 2 * py + pc) * rows, rows), :]

        def copy(k, blk, to, src=None):
            return pltpu.make_async_remote_copy(
                src_ref=block(*blk) if src is None else src, dst_ref=block(*blk), send_sem=send_sems.at[k],
                recv_sem=recv_sems.at[k], device_id=to, device_id_type=MESH)

        all_ref[pl.ds((4 * x + 2 * y + c) * rows, rows), :] = x_ref[...]
        first = [copy(0, me, sibling, src=x_ref)]
        first += [copy(1 + j, me, (*chip, c), src=x_ref) for j, chip in enumerate(chips)]
        for cp in first:
            cp.start()
        passed = [copy(4 + j, (*chip, c), sibling) for j, chip in enumerate(chips)]
        for j, chip in enumerate(chips):
            copy(1 + j, (*chip, c), me).wait_recv()
            passed[j].start()
        copy(0, sibling, me).wait_recv()
        for j, chip in enumerate(chips):
            copy(4 + j, (*chip, 1 - c), me).wait_recv()
        for cp in first + passed:
            cp.wait_send()
        total = all_ref[0:rows, :]
        for d in range(1, 8):
            total = total + all_ref[d * rows:(d + 1) * rows, :]
        out_ref[...] = total

    vm = pl.BlockSpec(memory_space=pltpu.VMEM)
    return _pcall(
        body, name="sum_small", in_specs=[vm], out_specs=vm, out_shape=jax.ShapeDtypeStruct((rows, width), F32),
        scratch_shapes=[pltpu.VMEM((8 * rows, width), F32), pltpu.SemaphoreType.DMA((7,)), pltpu.SemaphoreType.DMA((7,))],
    )(part)


def _row_tile(R, C, itemsize=4, budget=1 << 20):
    for t in (512, 256, 128, 64, 32, 16, 8):
        if R % t == 0 and t * C * itemsize <= budget:
            return t
    return R


def _add_halves(g, recv, c, *, name):
    _, R, C = g.shape
    half = R // 2
    t = _row_tile(half, C)
    nb = half // t

    def body(c_ref, g_ref, r_ref, o_ref):
        o_ref[...] = (g_ref[...].astype(F32) + r_ref[...].astype(F32)).astype(o_ref.dtype)

    grid_spec = pltpu.PrefetchScalarGridSpec(
        num_scalar_prefetch=1, grid=(4, nb),
        in_specs=[pl.BlockSpec((1, t, C), lambda k, i, cr: (k, cr[0] * nb + i, 0)),
                  pl.BlockSpec((1, t, C), lambda k, i, cr: (k, i, 0))],
        out_specs=pl.BlockSpec((1, t, C), lambda k, i, cr: (k, i, 0)))
    return _pcall(body, name=name, grid_spec=grid_spec, out_shape=jax.ShapeDtypeStruct((4, half, C), g.dtype),
                  compiler_params=_params("parallel", "parallel"))(c, g, recv)


def _add_owners(mine, recv, *, name):
    half, C = mine.shape
    t = _row_tile(half, C)

    def body(m_ref, r_ref, o_ref):
        o_ref[...] = ((m_ref[...].astype(F32) + r_ref[0].astype(F32)) + r_ref[1].astype(F32)) + r_ref[2].astype(F32)

    return _pcall(body, name=name, grid=(half // t,),
                  in_specs=[pl.BlockSpec((t, C), lambda i: (i, 0)), pl.BlockSpec((3, t, C), lambda i: (0, i, 0))],
                  out_specs=pl.BlockSpec((t, C), lambda i: (i, 0)), out_shape=jax.ShapeDtypeStruct((half, C), F32),
                  compiler_params=_params("parallel"))(mine, recv)


def _adamw(w, g, m, v, *, name):
    R, C = w.shape
    t = _row_tile(R, C)
    c1 = 1.0 - ADAM_B1 ** ADAM_STEP
    c2 = 1.0 - ADAM_B2 ** ADAM_STEP

    def body(w_ref, g_ref, m_ref, v_ref, d_ref, nm_ref, nv_ref):
        gv = g_ref[...]
        mn = ADAM_B1 * m_ref[...] + (1.0 - ADAM_B1) * gv
        vn = ADAM_B2 * v_ref[...] + (1.0 - ADAM_B2) * (gv * gv)
        d_ref[...] = -ADAM_LR * ((mn / c1) / (jnp.sqrt(vn / c2) + ADAM_EPS) + ADAM_WD * w_ref[...])
        nm_ref[...] = mn
        nv_ref[...] = vn

    blk = pl.BlockSpec((t, C), lambda i: (i, 0))
    shp = jax.ShapeDtypeStruct((R, C), F32)
    return _pcall(body, name=name, grid=(R // t,), in_specs=[blk] * 4, out_specs=[blk] * 3, out_shape=[shp] * 3,
                  compiler_params=_params("parallel"))(w, g, m, v)


BIG = ("w_in", "w_dil_out", "w_fox_out", "w_out", "w_ffn_in", "w_ffn_down")
SMALL = ("norm_mix_g", "b_fgt", "b_gate", "norm_ffn_g", "norm_final_g")
ORDER = ("norm_mix_g", "w_in", "b_fgt", "b_gate", "w_dil_out", "w_fox_out", "w_out", "norm_ffn_g", "w_ffn_in",
         "w_ffn_down", "norm_final_g")
SMALL_ROWS = {"norm_mix_g": (0, 1), "b_gate": (1, 3), "norm_ffn_g": (3, 4), "norm_final_g": (4, 5), "b_fgt": (5, 6)}


def _columns_to_blocks(full, ncol):
    K = full.shape[0]
    return full.reshape(K, 4, ncol).transpose(1, 0, 2)


def _blocks_to_columns(blocks):
    n, K, ncol = blocks.shape
    return blocks.transpose(1, 0, 2).reshape(K, n * ncol)


def kernel(x, norm_mix_g, w_in, b_fgt, b_gate, w_dil_out, w_fox_out, w_out, norm_ffn_g, w_ffn_in, w_ffn_down, norm_final_g, loss_target, m_norm_mix_g, m_w_in, m_b_fgt, m_b_gate, m_w_dil_out, m_w_fox_out, m_w_out, m_norm_ffn_g, m_w_ffn_in, m_w_ffn_down, m_norm_final_g, v_norm_mix_g, v_w_in, v_b_fgt, v_b_gate, v_w_dil_out, v_w_fox_out, v_w_out, v_norm_ffn_g, v_w_ffn_in, v_w_ffn_down, v_norm_final_g):
    weights = dict(norm_mix_g=norm_mix_g, w_in=w_in, b_fgt=b_fgt, b_gate=b_gate, w_dil_out=w_dil_out,
                   w_fox_out=w_fox_out, w_out=w_out, norm_ffn_g=norm_ffn_g, w_ffn_in=w_ffn_in, w_ffn_down=w_ffn_down,
                   norm_final_g=norm_final_g)
    m_in = dict(norm_mix_g=m_norm_mix_g, w_in=m_w_in, b_fgt=m_b_fgt, b_gate=m_b_gate, w_dil_out=m_w_dil_out,
                w_fox_out=m_w_fox_out, w_out=m_w_out, norm_ffn_g=m_norm_ffn_g, w_ffn_in=m_w_ffn_in,
                w_ffn_down=m_w_ffn_down, norm_final_g=m_norm_final_g)
    v_in = dict(norm_mix_g=v_norm_mix_g, w_in=v_w_in, b_fgt=v_b_fgt, b_gate=v_b_gate, w_dil_out=v_w_dil_out,
                w_fox_out=v_w_fox_out, w_out=v_w_out, norm_ffn_g=v_norm_ffn_g, w_ffn_in=v_w_ffn_in,
                w_ffn_down=v_w_ffn_down, norm_final_g=v_norm_final_g)
    c = lax.axis_index("c")
    chip = 2 * lax.axis_index("x") + lax.axis_index("y")

    shards = {n: weights[n][0].astype(_CD) for n in BIG}
    (g_in,) = _gather_weights([shards["w_in"]])
    late = BIG[1:]
    send_g, recv_g, late_src, late_land, token = _split_copy_start(
        [shards[n] for n in late], [jax.ShapeDtypeStruct((4,) + shards[n].shape, _CD) for n in late],
        _gather_copies, g_in, name="gather_late_start")
    full_in = _blocks_to_columns(lax.dynamic_update_index_in_dim(g_in, shards["w_in"], chip, 0))
    o3 = QKV_COLS
    o4 = o3 + N_FOX_HEADS
    w = dict(qkv=full_in[:, :o3], f=jnp.pad(full_in[:, o3:o4], ((0, 0), (0, F_PAD - N_FOX_HEADS))), g=full_in[:, o4:])
    p = dict(norm_mix_g=norm_mix_g, b_fgt=jnp.pad(b_fgt, ((0, 0), (0, F_PAD - N_FOX_HEADS))), b_gate=b_gate,
             norm_ffn_g=norm_ffn_g, norm_final_g=norm_final_g.reshape(1, D_MODEL))

    def late_weights(after):
        own, lands = _split_copy_wait(send_g, recv_g, late_src, late_land, _gather_copies, after,
                                      name="gather_late_wait")
        lands = _forward_halves(lands, name="gather_late_forward")
        g_dil, g_fox, g_out, g_ffn_in, g_ffn_down = [
            lax.dynamic_update_index_in_dim(l, s, chip, 0) for l, s in zip(lands, own)]
        return dict(dil_out=_blocks_to_columns(g_dil), fox_out=_blocks_to_columns(g_fox),
                    out=g_out.reshape(D_MODEL, D_MODEL), ffn_in=g_ffn_in,
                    ffn_down=g_ffn_down.reshape(D_FF, D_MODEL))

    c_arr = jnp.reshape(c, (1,)).astype(jnp.int32)

    def to_blocks(n, full):
        shape = weights[n].shape
        if full.ndim == 3:
            return full
        if n in ("w_out", "w_ffn_down"):
            return full.reshape(4, shape[1], shape[2])
        return _columns_to_blocks(full, shape[2])

    def pair_sums(group, named):
        names = list(named)
        blocks = [to_blocks(n, named[n]) for n in names]
        from_sibling = _swap_halves(blocks, name=f"swap_halves_{group}")
        return [_add_halves(b, r, c_arr, name=f"add_halves_{n}") for b, r, n in zip(blocks, from_sibling, names)]

    in_flight = {}

    def grad_sink(group, gw):
        if group == "in":
            named = {"w_in": jnp.concatenate([gw["qkv"], gw["f"][:, :N_FOX_HEADS], gw["g"]], axis=1)}
        else:
            named = {"w_" + k: v for k, v in gw.items()}
        sums = pair_sums(group, named)
        started = _split_copy_start(sums, [jax.ShapeDtypeStruct((3,) + s.shape[1:], s.dtype) for s in sums],
                                    _scatter_copies, next(iter(gw.values())), name=f"scatter_{group}_start")
        in_flight[group] = (list(named), started)
        return started[-1]

    loss_part, grad_x, gw, small = _layer_step(x[0], loss_target[0], w, p, late_weights, grad_sink, token)

    def owner_sums(names, sums, from_chips):
        return {n: _add_owners(lax.dynamic_index_in_dim(s, chip, 0, keepdims=False), r, name=f"add_owners_{n}")
                for n, s, r in zip(names, sums, from_chips)}

    halves = {}
    for group, (names, (send_s, recv_s, srcs, lands, _)) in in_flight.items():
        sums, from_chips = _split_copy_wait(send_s, recv_s, srcs, lands, _scatter_copies, grad_x,
                                            name=f"scatter_{group}_wait")
        halves.update(owner_sums(names, sums, from_chips))
    halves = [halves[n] for n in BIG]
    grads = {}
    for n, own, other in zip(BIG, halves, _share_halves(halves)):
        pair = jnp.stack([own, other])
        grads[n] = jnp.where(c == 0, pair, pair[::-1]).reshape(2 * own.shape[0], own.shape[1])

    packed = jnp.concatenate([
        small["norm_mix_g"], small["b_gate"].reshape(2, D_MODEL), small["norm_ffn_g"], small["norm_final_g"],
        jnp.pad(small["b_fgt"], ((0, 0), (0, D_MODEL - F_PAD))), jnp.zeros((2, D_MODEL), F32)], axis=0)
    summed = _sum_small(packed)
    for n in SMALL:
        lo, hi = SMALL_ROWS[n]
        grads[n] = summed[lo:hi].reshape(1, -1)[:, :weights[n].size]

    loss = lax.psum(loss_part[0, 0], ("x", "y", "c"))

    out_g, out_d, out_m, out_v = {}, {}, {}, {}
    for n in ORDER:
        shape = weights[n].shape
        two_d = shape[1:] if len(shape) == 3 else (1, weights[n].size)
        g2 = grads[n].reshape(two_d)
        d2, m2, v2 = _adamw(weights[n].reshape(two_d), g2, m_in[n].reshape(two_d), v_in[n].reshape(two_d),
                            name=f"adamw_{n}")
        out_g[n], out_d[n], out_m[n], out_v[n] = (g2.reshape(shape), d2.reshape(shape), m2.reshape(shape),
                                                  v2.reshape(shape))
    return (loss, grad_x[None], *[out_g[n] for n in ORDER], *[out_d[n] for n in ORDER],
            *[out_m[n] for n in ORDER], *[out_v[n] for n in ORDER])
```

```python
import numpy as np
import jax
import jax.numpy as jnp
from jax import lax
from jax.experimental import pallas as pl
from jax.experimental.pallas import tpu as pltpu

F32 = jnp.float32
_CD = jnp.bfloat16

D_MODEL = 1024
HEAD_DIM = 64
DIL_PAIRS = ((128, 1), (512, 4), (2048, 16))
N_DIL_GROUPS = 3
DIL_HEADS = 4
DIL_W = 128
DIL_OUT = DIL_HEADS * HEAD_DIM
DIL_WIDTH = N_DIL_GROUPS * DIL_OUT
N_FOX_HEADS = 8
FOX_WIDTH = N_FOX_HEADS * HEAD_DIM
D_FF = 2816
QKV_COLS = 3 * DIL_WIDTH + 3 * FOX_WIDTH
F_PAD = 128
RMS_EPS = 1e-6
NEG_INF = -1e30
ATTN_SCALE = HEAD_DIM ** -0.5
ADAM_LR, ADAM_B1, ADAM_B2, ADAM_EPS, ADAM_WD, ADAM_STEP = 0.001, 0.9, 0.999, 1e-08, 0.01, 10

VMEM_LIMIT = 48 * 1024 * 1024
MESH = pl.DeviceIdType.MESH
HBM_SPEC = pl.BlockSpec(memory_space=pltpu.HBM)


def _pcall(body, after=None, **kw):
    if after is None:
        return pl.pallas_call(body, **kw)
    n_in = len(kw["in_specs"])
    kw["in_specs"] = list(kw["in_specs"]) + [pl.BlockSpec(memory_space=pl.ANY)]

    def tied(*refs):
        return body(*refs[:n_in], *refs[n_in + 1:])

    call = pl.pallas_call(tied, **kw)
    return lambda *args: call(*args, after)


def _params(*sem):
    return pltpu.CompilerParams(dimension_semantics=sem, vmem_limit_bytes=VMEM_LIMIT)


def _pick(dim, pref):
    t = (min(pref, dim) // 128) * 128
    while t >= 128:
        if dim % t == 0:
            return t
        t -= 128
    return dim


def _mm(a, b, *, name, ta=False, tb=False, out_dtype=F32, add=None, tm=1024, tn=512, tk=2048, after=None,
        b_blocks=False, out_blocks=None):
    if ta:
        K, M = a.shape
    else:
        M, K = a.shape
    b_rows, b_cols = (b.shape[1], b.shape[0] * b.shape[2]) if b_blocks else b.shape
    if tb:
        N, K2 = b_rows, b_cols
    else:
        K2, N = b_rows, b_cols
    assert K == K2, (a.shape, b.shape)
    shard = b.shape[2] if b_blocks else None
    tm = _pick(M, tm)
    tn = _pick(shard if (b_blocks and not tb) else (out_blocks or N), tn)
    tk = _pick(shard if (b_blocks and tb) else K, tk)
    nk = K // tk
    dn = (((0 if ta else 1,), (1 if tb else 0,)), ((), ()))
    has_add = add is not None
    assert not (has_add and out_blocks)

    def body(*refs):
        a_ref, b_ref = refs[0], refs[1]
        add_ref = refs[2] if has_add else None
        o_ref = refs[3] if has_add else refs[2]
        bv = b_ref[0] if b_blocks else b_ref[...]
        p = lax.dot_general(a_ref[...].astype(_CD), bv.astype(_CD), dn, preferred_element_type=F32)

        def finish(r):
            if has_add:
                r = r + add_ref[...]
            if out_blocks:
                o_ref[0] = r.astype(out_dtype)
            else:
                o_ref[...] = r.astype(out_dtype)

        if nk == 1:
            finish(p)
        else:
            acc_ref = refs[-1]
            k = pl.program_id(2)

            @pl.when(k == 0)
            def _():
                acc_ref[...] = p

            @pl.when(k > 0)
            def _():
                acc_ref[...] += p

            @pl.when(k == nk - 1)
            def _():
                finish(acc_ref[...])

    a_spec = pl.BlockSpec((tk, tm), lambda i, j, k: (k, i)) if ta else pl.BlockSpec((tm, tk), lambda i, j, k: (i, k))
    if b_blocks and tb:
        per = shard // tk
        b_spec = pl.BlockSpec((1, tn, tk), lambda i, j, k: (k // per, j, k % per))
    elif b_blocks:
        per = shard // tn
        b_spec = pl.BlockSpec((1, tk, tn), lambda i, j, k: (j // per, k, j % per))
    else:
        b_spec = pl.BlockSpec((tn, tk), lambda i, j, k: (j, k)) if tb else pl.BlockSpec((tk, tn), lambda i, j, k: (k, j))
    if out_blocks:
        oper = out_blocks // tn
        o_spec = pl.BlockSpec((1, tm, tn), lambda i, j, k: (j // oper, i, j % oper))
        out_shape = jax.ShapeDtypeStruct((N // out_blocks, M, out_blocks), out_dtype)
    else:
        o_spec = pl.BlockSpec((tm, tn), lambda i, j, k: (i, j))
        out_shape = jax.ShapeDtypeStruct((M, N), out_dtype)
    in_specs = [a_spec, b_spec] + ([o_spec] if has_add else [])
    args = (a, b) + ((add,) if has_add else ())
    return _pcall(
        body, after, name=name, grid=(M // tm, N // tn, nk), in_specs=in_specs, out_specs=o_spec,
        out_shape=out_shape,
        scratch_shapes=[pltpu.VMEM((tm, tn), F32)] if nk > 1 else [],
        compiler_params=_params("parallel", "parallel", "arbitrary"),
    )(*args)


def _rms_fwd(x, g, *, name, tm=512, after=None):
    S, D = x.shape

    def body(x_ref, g_ref, h_ref):
        xv = x_ref[...]
        r = lax.rsqrt(jnp.mean(xv * xv, axis=-1, keepdims=True) + RMS_EPS)
        h_ref[...] = ((xv * r) * g_ref[...]).astype(h_ref.dtype)

    row = pl.BlockSpec((tm, D), lambda i: (i, 0))
    return _pcall(body, after, name=name, grid=(S // tm,), in_specs=[row, pl.BlockSpec((1, D), lambda i: (0, 0))],
                  out_specs=row, out_shape=jax.ShapeDtypeStruct((S, D), _CD), compiler_params=_params("parallel"))(x, g)


def _rms_bwd(x, g, dh, dres, *, name, tm=512, after=None):
    S, D = x.shape

    def body(x_ref, g_ref, dh_ref, dres_ref, dx_ref, dg_ref):
        xv = x_ref[...]
        r = lax.rsqrt(jnp.mean(xv * xv, axis=-1, keepdims=True) + RMS_EPS)
        xh = xv * r
        dhv = dh_ref[...]
        dxh = dhv * g_ref[...]
        dx_ref[...] = dres_ref[...] + r * (dxh - xh * jnp.mean(dxh * xh, axis=-1, keepdims=True))
        part = jnp.sum(dhv * xh, axis=0, keepdims=True)

        @pl.when(pl.program_id(0) == 0)
        def _():
            dg_ref[...] = part

        @pl.when(pl.program_id(0) > 0)
        def _():
            dg_ref[...] += part

    row = pl.BlockSpec((tm, D), lambda i: (i, 0))
    vec = pl.BlockSpec((1, D), lambda i: (0, 0))
    return _pcall(body, after, name=name, grid=(S // tm,), in_specs=[row, vec, row, row], out_specs=[row, vec],
                  out_shape=[jax.ShapeDtypeStruct((S, D), F32), jax.ShapeDtypeStruct((1, D), F32)],
                  compiler_params=_params("arbitrary"))(x, g, dh, dres)


def _loss_head(x, g, tgt, *, name, tm=512):
    S, D = x.shape

    def body(x_ref, g_ref, t_ref, loss_ref, dx_ref, dg_ref):
        xv = x_ref[...]
        gv = g_ref[...]
        r = lax.rsqrt(jnp.mean(xv * xv, axis=-1, keepdims=True) + RMS_EPS)
        xh = xv * r
        err = xh * gv - t_ref[...]
        lpart = 0.5 * jnp.sum(jnp.mean(err * err, axis=-1, keepdims=True), axis=0, keepdims=True)
        dy = err * (1.0 / D)
        dxh = dy * gv
        dx_ref[...] = r * (dxh - xh * jnp.mean(dxh * xh, axis=-1, keepdims=True))
        gpart = jnp.sum(dy * xh, axis=0, keepdims=True)

        @pl.when(pl.program_id(0) == 0)
        def _():
            loss_ref[...] = lpart
            dg_ref[...] = gpart

        @pl.when(pl.program_id(0) > 0)
        def _():
            loss_ref[...] += lpart
            dg_ref[...] += gpart

    row = pl.BlockSpec((tm, D), lambda i: (i, 0))
    vec = pl.BlockSpec((1, D), lambda i: (0, 0))
    one = pl.BlockSpec((1, 1), lambda i: (0, 0))
    return _pcall(body, name=name, grid=(S // tm,), in_specs=[row, vec, row], out_specs=[one, row, vec],
                  out_shape=[jax.ShapeDtypeStruct((1, 1), F32), jax.ShapeDtypeStruct((S, D), F32),
                             jax.ShapeDtypeStruct((1, D), F32)],
                  compiler_params=_params("arbitrary"))(x, g, tgt)


def _sigmoid(z):
    return 1.0 / (1.0 + jnp.exp(-z))


def _gate_fwd(gl, bg, ya, yb, *, name, tm=512):
    S, D = ya.shape

    def body(za_ref, zb_ref, ba_ref, bb_ref, ya_ref, yb_ref, o_ref):
        ga = _sigmoid(za_ref[...] + ba_ref[...])
        gb = _sigmoid(zb_ref[...] + bb_ref[...])
        o_ref[...] = (ga * ya_ref[...] + gb * yb_ref[...]).astype(o_ref.dtype)

    lo = pl.BlockSpec((tm, D), lambda i: (i, 0))
    hi = pl.BlockSpec((tm, D), lambda i: (i, 1))
    vlo = pl.BlockSpec((1, D), lambda i: (0, 0))
    vhi = pl.BlockSpec((1, D), lambda i: (0, 1))
    return _pcall(body, name=name, grid=(S // tm,), in_specs=[lo, hi, vlo, vhi, lo, lo], out_specs=lo,
                  out_shape=jax.ShapeDtypeStruct((S, D), _CD), compiler_params=_params("parallel"))(gl, gl, bg, bg, ya, yb)


def _gate_bwd(dm, gl, bg, ya, yb, *, name, tm=256):
    S, D = ya.shape

    def body(dm_ref, za_ref, zb_ref, ba_ref, bb_ref, ya_ref, yb_ref, dya_ref, dyb_ref, dgl_ref, dbg_ref):
        dmv = dm_ref[...]
        ga = _sigmoid(za_ref[...] + ba_ref[...])
        gb = _sigmoid(zb_ref[...] + bb_ref[...])
        dya_ref[...] = (dmv * ga).astype(dya_ref.dtype)
        dyb_ref[...] = (dmv * gb).astype(dyb_ref.dtype)
        dza = dmv * ya_ref[...] * ga * (1.0 - ga)
        dzb = dmv * yb_ref[...] * gb * (1.0 - gb)
        dgl_ref[:, :D] = dza.astype(dgl_ref.dtype)
        dgl_ref[:, D:] = dzb.astype(dgl_ref.dtype)
        pa = jnp.sum(dza, axis=0, keepdims=True)
        pb = jnp.sum(dzb, axis=0, keepdims=True)

        @pl.when(pl.program_id(0) == 0)
        def _():
            dbg_ref[:, :D] = pa
            dbg_ref[:, D:] = pb

        @pl.when(pl.program_id(0) > 0)
        def _():
            dbg_ref[:, :D] += pa
            dbg_ref[:, D:] += pb

    lo = pl.BlockSpec((tm, D), lambda i: (i, 0))
    hi = pl.BlockSpec((tm, D), lambda i: (i, 1))
    vlo = pl.BlockSpec((1, D), lambda i: (0, 0))
    vhi = pl.BlockSpec((1, D), lambda i: (0, 1))
    wide = pl.BlockSpec((tm, 2 * D), lambda i: (i, 0))
    vwide = pl.BlockSpec((1, 2 * D), lambda i: (0, 0))
    return _pcall(body, name=name, grid=(S // tm,), in_specs=[lo, lo, hi, vlo, vhi, lo, lo],
                  out_specs=[lo, lo, wide, vwide],
                  out_shape=[jax.ShapeDtypeStruct((S, D), _CD), jax.ShapeDtypeStruct((S, D), _CD),
                             jax.ShapeDtypeStruct((S, 2 * D), _CD), jax.ShapeDtypeStruct((1, 2 * D), F32)],
                  compiler_params=_params("arbitrary"))(dm, gl, gl, bg, bg, ya, yb)


def _swiglu_fwd(gu, *, name, tm=256):
    S, F2 = gu.shape
    F = F2 // 2

    def body(g_ref, u_ref, o_ref):
        gv = g_ref[...].astype(F32)
        o_ref[...] = (gv * _sigmoid(gv) * u_ref[...].astype(F32)).astype(o_ref.dtype)

    lo = pl.BlockSpec((tm, F), lambda i: (i, 0))
    hi = pl.BlockSpec((tm, F), lambda i: (i, 1))
    return _pcall(body, name=name, grid=(S // tm,), in_specs=[lo, hi], out_specs=lo,
                  out_shape=jax.ShapeDtypeStruct((S, F), _CD), compiler_params=_params("parallel"))(gu, gu)


def _swiglu_bwd(dact, gu, *, name, tm=256):
    S, F2 = gu.shape
    F = F2 // 2

    def body(d_ref, g_ref, u_ref, o_ref):
        dv = d_ref[...].astype(F32)
        gv = g_ref[...].astype(F32)
        sg = _sigmoid(gv)
        o_ref[:, :F] = (dv * u_ref[...].astype(F32) * (sg * (1.0 + gv * (1.0 - sg)))).astype(o_ref.dtype)
        o_ref[:, F:] = (dv * (gv * sg)).astype(o_ref.dtype)

    lo = pl.BlockSpec((tm, F), lambda i: (i, 0))
    hi = pl.BlockSpec((tm, F), lambda i: (i, 1))
    return _pcall(body, name=name, grid=(S // tm,), in_specs=[lo, lo, hi],
                  out_specs=pl.BlockSpec((tm, F2), lambda i: (i, 0)),
                  out_shape=jax.ShapeDtypeStruct((S, F2), _CD), compiler_params=_params("parallel"))(dact, gu, gu)


def _split3(x):
    hi = x.astype(jnp.bfloat16)
    r1 = x - hi.astype(F32)
    mid = r1.astype(jnp.bfloat16)
    lo = (r1 - mid.astype(F32)).astype(jnp.bfloat16)
    return hi, mid, lo


def _ones_dot_left(ones, x):
    return sum(jnp.dot(ones, p, preferred_element_type=F32) for p in _split3(x))


def _ones_dot_right(x, ones):
    return sum(jnp.dot(p, ones, preferred_element_type=F32) for p in _split3(x))


def _head_sum(x):
    n = x.shape[1]
    r = lax.broadcasted_iota(jnp.int32, (n, n), 0) // HEAD_DIM
    c = lax.broadcasted_iota(jnp.int32, (n, n), 1) // HEAD_DIM
    return _ones_dot_right(x, (r == c).astype(jnp.bfloat16))


def _log_sigmoid(z):
    e = jnp.exp(-jnp.abs(z))
    t = 1.0 + e
    log1p_e = jnp.where(t == 1.0, e, jnp.log(t) * (e / jnp.where(t == 1.0, 1.0, t - 1.0)))
    return jnp.minimum(z, 0.0) - log1p_e


def _fox_cumsum(zf, bf, *, name):
    S, W = zf.shape
    nb = S // 128

    def body(z_ref, b_ref, c_ref):
        tri = (lax.broadcasted_iota(jnp.int32, (128, 128), 0) >= lax.broadcasted_iota(jnp.int32, (128, 128), 1))
        tri = tri.astype(jnp.bfloat16)

        def step(i, carry):
            rows = pl.ds(pl.multiple_of(i * 128, 128), 128)
            lf = _log_sigmoid(z_ref[rows, :] + b_ref[...])
            cb = _ones_dot_left(tri, lf) + carry
            c_ref[rows, :] = cb
            return cb[127:128, :]

        lax.fori_loop(0, nb, step, jnp.zeros((1, W), F32))

    return _pcall(body, name=name, out_shape=jax.ShapeDtypeStruct((S, W), F32),
                  compiler_params=pltpu.CompilerParams(vmem_limit_bytes=VMEM_LIMIT))(zf, bf)


def _fox_cumsum_bwd(dc, zf, bf, *, name):
    S, W = zf.shape
    nb = S // 128

    def body(dc_ref, z_ref, b_ref, dz_ref, db_ref):
        tri = (lax.broadcasted_iota(jnp.int32, (128, 128), 0) <= lax.broadcasted_iota(jnp.int32, (128, 128), 1))
        tri = tri.astype(jnp.bfloat16)

        def step(k, carry):
            tail, acc = carry
            i = nb - 1 - k
            rows = pl.ds(pl.multiple_of(i * 128, 128), 128)
            dlf = _ones_dot_left(tri, dc_ref[rows, :]) + tail
            dz = dlf * _sigmoid(-(z_ref[rows, :] + b_ref[...]))
            dz_ref[rows, :] = dz
            return dlf[0:1, :], acc + jnp.sum(dz, axis=0, keepdims=True)

        _, acc = lax.fori_loop(0, nb, step, (jnp.zeros((1, W), F32), jnp.zeros((1, W), F32)))
        db_ref[...] = acc

    return _pcall(body, name=name,
                  out_shape=[jax.ShapeDtypeStruct((S, W), F32), jax.ShapeDtypeStruct((1, W), F32)],
                  compiler_params=pltpu.CompilerParams(vmem_limit_bytes=VMEM_LIMIT))(dc, zf, bf)


def _dil_slopes(group):
    h = np.arange(1, N_DIL_GROUPS * DIL_HEADS + 1, dtype=np.float32)
    s = (np.float32(2.0) ** (np.float32(-8.0) * h / np.float32(N_DIL_GROUPS * DIL_HEADS))).astype(np.float32)
    return [float(v) for v in s.reshape(N_DIL_GROUPS, DIL_HEADS)[group]]


def _dil_tiles(i, n, blocks_per_seq):
    qi = lax.broadcasted_iota(jnp.int32, (DIL_W, DIL_W), 0)
    kj = lax.broadcasted_iota(jnp.int32, (DIL_W, DIL_W), 1)
    first = ((4 * n + i) % blocks_per_seq) == 0
    valid_prev = jnp.logical_and(kj >= qi, jnp.logical_not(first))
    valid_cur = kj <= qi
    rel_prev = (qi - kj + DIL_W).astype(F32)
    rel_cur = (qi - kj).astype(F32)
    return valid_prev, valid_cur, rel_prev, rel_cur


CHUNK = 4 * DIL_W


def _dil_fwd(q, k, v, group, *, name):
    S = q.shape[0]
    dilation = DIL_PAIRS[group][1]
    bps = (S // dilation) // DIL_W
    slopes = _dil_slopes(group)
    nt = (((1,), (1,)), ((), ()))

    def body(q_ref, k_ref, v_ref, kp_ref, vp_ref, o_ref, l_ref):
        n = pl.program_id(0)
        for i in range(4):
            valid_prev, valid_cur, rel_prev, rel_cur = _dil_tiles(i, n, bps)
            rows = slice(i * DIL_W, (i + 1) * DIL_W)
            prow = slice((i - 1) * DIL_W, i * DIL_W)
            for h in range(DIL_HEADS):
                cols = slice(h * HEAD_DIM, (h + 1) * HEAD_DIM)
                qh = q_ref[rows, cols]
                kc, vc = k_ref[rows, cols], v_ref[rows, cols]
                kp = kp_ref[:, cols] if i == 0 else k_ref[prow, cols]
                vp = vp_ref[:, cols] if i == 0 else v_ref[prow, cols]
                sl = slopes[h] * dilation
                sp = lax.dot_general(qh, kp, nt, preferred_element_type=F32) * ATTN_SCALE - sl * rel_prev
                sc = lax.dot_general(qh, kc, nt, preferred_element_type=F32) * ATTN_SCALE - sl * rel_cur
                sp = jnp.where(valid_prev, sp, NEG_INF)
                sc = jnp.where(valid_cur, sc, NEG_INF)
                m = jnp.maximum(jnp.max(sp, axis=-1, keepdims=True), jnp.max(sc, axis=-1, keepdims=True))
                pp, pc = jnp.exp(sp - m), jnp.exp(sc - m)
                den = jnp.sum(pp, axis=-1, keepdims=True) + jnp.sum(pc, axis=-1, keepdims=True)
                acc = (jnp.dot(pp.astype(_CD), vp, preferred_element_type=F32)
                       + jnp.dot(pc.astype(_CD), vc, preferred_element_type=F32))
                o_ref[rows, cols] = acc / den
                l_ref[rows, cols] = jnp.broadcast_to(m + jnp.log(den), (DIL_W, HEAD_DIM))

    cur = pl.BlockSpec((CHUNK, DIL_OUT), lambda n: (n, 0))
    prev = pl.BlockSpec((DIL_W, DIL_OUT), lambda n: (jnp.maximum(4 * n - 1, 0), 0))
    return _pcall(body, name=name, grid=(S // CHUNK,), in_specs=[cur, cur, cur, prev, prev], out_specs=[cur, cur],
                  out_shape=[jax.ShapeDtypeStruct((S, DIL_OUT), F32), jax.ShapeDtypeStruct((S, DIL_OUT), F32)],
                  compiler_params=_params("parallel"))(q, k, v, k, v)


def _dil_bwd(q, k, v, o, lse, do, dlse, group, *, name):
    S = q.shape[0]
    dilation = DIL_PAIRS[group][1]
    bps = (S // dilation) // DIL_W
    slopes = _dil_slopes(group)
    nchunk = S // CHUNK
    nt = (((1,), (1,)), ((), ()))
    tn = (((0,), (0,)), ((), ()))

    def body(q_ref, k_ref, v_ref, kp_ref, vp_ref, o_ref, l_ref, do_ref, dl_ref, dq_ref, dk_ref, dv_ref,
             dk_s, dv_s):
        step = pl.program_id(0)
        n = nchunk - 1 - step

        @pl.when(step == 0)
        def _():
            dk_s[CHUNK:, :] = jnp.zeros((DIL_W, DIL_OUT), F32)
            dv_s[CHUNK:, :] = jnp.zeros((DIL_W, DIL_OUT), F32)

        dk_s[:CHUNK, :] = jnp.zeros((CHUNK, DIL_OUT), F32)
        dv_s[:CHUNK, :] = jnp.zeros((CHUNK, DIL_OUT), F32)
        for i in range(4):
            valid_prev, valid_cur, rel_prev, rel_cur = _dil_tiles(i, n, bps)
            rows = slice(i * DIL_W, (i + 1) * DIL_W)
            prow = slice((i - 1) * DIL_W, i * DIL_W)
            s_prev = slice(i * DIL_W, (i + 1) * DIL_W)
            s_cur = slice((i + 1) * DIL_W, (i + 2) * DIL_W)
            for h in range(DIL_HEADS):
                cols = slice(h * HEAD_DIM, (h + 1) * HEAD_DIM)
                qh = q_ref[rows, cols]
                kc, vc = k_ref[rows, cols], v_ref[rows, cols]
                kp = kp_ref[:, cols] if i == 0 else k_ref[prow, cols]
                vp = vp_ref[:, cols] if i == 0 else v_ref[prow, cols]
                sl = slopes[h] * dilation
                lh = l_ref[rows, h * HEAD_DIM:h * HEAD_DIM + 1]
                sp = lax.dot_general(qh, kp, nt, preferred_element_type=F32) * ATTN_SCALE - sl * rel_prev
                sc = lax.dot_general(qh, kc, nt, preferred_element_type=F32) * ATTN_SCALE - sl * rel_cur
                pp = jnp.exp(jnp.where(valid_prev, sp, NEG_INF) - lh)
                pc = jnp.exp(jnp.where(valid_cur, sc, NEG_INF) - lh)
                doh = do_ref[rows, cols]
                dsum = jnp.sum(doh * o_ref[rows, cols], axis=-1, keepdims=True)
                shift = dl_ref[rows, h * HEAD_DIM:h * HEAD_DIM + 1] - dsum
                dob = doh.astype(_CD)
                dsp = pp * (lax.dot_general(dob, vp, nt, preferred_element_type=F32) + shift)
                dsc = pc * (lax.dot_general(dob, vc, nt, preferred_element_type=F32) + shift)
                dspb = (dsp * ATTN_SCALE).astype(_CD)
                dscb = (dsc * ATTN_SCALE).astype(_CD)
                dq_ref[rows, cols] = (jnp.dot(dspb, kp, preferred_element_type=F32)
                                      + jnp.dot(dscb, kc, preferred_element_type=F32)).astype(dq_ref.dtype)
                dk_s[s_prev, cols] += lax.dot_general(dspb, qh, tn, preferred_element_type=F32)
                dk_s[s_cur, cols] += lax.dot_general(dscb, qh, tn, preferred_element_type=F32)
                dv_s[s_prev, cols] += lax.dot_general(pp.astype(_CD), dob, tn, preferred_element_type=F32)
                dv_s[s_cur, cols] += lax.dot_general(pc.astype(_CD), dob, tn, preferred_element_type=F32)
        dk_ref[...] = dk_s[DIL_W:, :].astype(dk_ref.dtype)
        dv_ref[...] = dv_s[DIL_W:, :].astype(dv_ref.dtype)
        dk_s[CHUNK:, :] = dk_s[:DIL_W, :]
        dv_s[CHUNK:, :] = dv_s[:DIL_W, :]

    cur = pl.BlockSpec((CHUNK, DIL_OUT), lambda s: (nchunk - 1 - s, 0))
    prev = pl.BlockSpec((DIL_W, DIL_OUT), lambda s: (jnp.maximum(4 * (nchunk - 1 - s) - 1, 0), 0))
    shp = jax.ShapeDtypeStruct((S, DIL_OUT), _CD)
    return _pcall(body, name=name, grid=(nchunk,), in_specs=[cur, cur, cur, prev, prev, cur, cur, cur, cur],
                  out_specs=[cur, cur, cur], out_shape=[shp, shp, shp],
                  scratch_shapes=[pltpu.VMEM((CHUNK + DIL_W, DIL_OUT), F32), pltpu.VMEM((CHUNK + DIL_W, DIL_OUT), F32)],
                  compiler_params=_params("arbitrary"))(q, k, v, k, v, o, lse, do, dlse)


def _dil_mix_fwd(os_, ls_, *, name, tm=512):
    S, W = os_[0].shape

    def body(o0, o1, o2, l0, l1, l2, out_ref):
        ls = [l0[...], l1[...], l2[...]]
        m = jnp.maximum(jnp.maximum(ls[0], ls[1]), ls[2])
        es = [jnp.exp(l - m) for l in ls]
        den = es[0] + es[1] + es[2]
        out_ref[...] = ((es[0] * o0[...] + es[1] * o1[...] + es[2] * o2[...]) / den).astype(out_ref.dtype)

    row = pl.BlockSpec((tm, W), lambda i: (i, 0))
    return _pcall(body, name=name, grid=(S // tm,), in_specs=[row] * 6, out_specs=row,
                  out_shape=jax.ShapeDtypeStruct((S, W), _CD), compiler_params=_params("parallel"))(*os_, *ls_)


def _dil_mix_bwd(doa, os_, ls_, *, name, tm=512, after=None):
    S, W = doa.shape

    def body(d_ref, o0, o1, o2, l0, l1, l2, do0, do1, do2, dl0, dl1, dl2):
        dv = d_ref[...]
        ls = [l0[...], l1[...], l2[...]]
        m = jnp.maximum(jnp.maximum(ls[0], ls[1]), ls[2])
        es = [jnp.exp(l - m) for l in ls]
        den = es[0] + es[1] + es[2]
        al = [e / den for e in es]
        da = [_head_sum(dv * o[...]) for o in (o0, o1, o2)]
        mean = al[0] * da[0] + al[1] * da[1] + al[2] * da[2]
        for a, d_, do_ref, dl_ref in zip(al, da, (do0, do1, do2), (dl0, dl1, dl2)):
            do_ref[...] = a * dv
            dl_ref[...] = a * (d_ - mean)

    row = pl.BlockSpec((tm, W), lambda i: (i, 0))
    shp = jax.ShapeDtypeStruct((S, W), F32)
    return _pcall(body, after, name=name, grid=(S // tm,), in_specs=[row] * 7, out_specs=[row] * 6, out_shape=[shp] * 6,
                  compiler_params=_params("parallel"))(doa, *os_, *ls_)


FOX_T = 512


PACK = 2 * HEAD_DIM
HEAD_PAIRS = N_FOX_HEADS // 2
Q_BLOCK0 = (3 * DIL_WIDTH) // PACK
K_BLOCK0 = (3 * DIL_WIDTH + FOX_WIDTH) // PACK
V_BLOCK0 = (3 * DIL_WIDTH + 2 * FOX_WIDTH) // PACK


def _pieces(x):
    hi = x.astype(jnp.bfloat16).astype(F32)
    r = x - hi
    mid = r.astype(jnp.bfloat16).astype(F32)
    lo = (r - mid).astype(jnp.bfloat16).astype(F32)
    return [hi, mid, lo]


def _extras(first, second, rows):
    lane = lax.broadcasted_iota(jnp.int32, (rows, HEAD_DIM), 1)
    out = jnp.zeros((rows, HEAD_DIM), F32)
    for idx, val in enumerate(list(first) + list(second)):
        out = jnp.where(lane == idx, val, out)
    return out


def _head_column(c, h):
    lane = lax.broadcasted_iota(jnp.int32, c.shape, 1)
    return jnp.sum(jnp.where(lane == h, c, 0.0), axis=1, keepdims=True)


ONES3 = [1.0, 1.0, 1.0]
ZEROS3 = [0.0, 0.0, 0.0]


def _fox_pack_fwd(qkv, c, *, name, tm=512):
    S = qkv.shape[0]

    def body(q_ref, k_ref, v_ref, c_ref, qo_ref, ko_ref, vo_ref):
        hp = pl.program_id(1)
        cv = c_ref[...]
        for hh in range(2):
            ch = _pieces(_head_column(cv, 2 * hp + hh))
            src = slice(hh * HEAD_DIM, (hh + 1) * HEAD_DIM)
            lo = slice(hh * PACK, hh * PACK + HEAD_DIM)
            hi = slice(hh * PACK + HEAD_DIM, (hh + 1) * PACK)
            qo_ref[:, lo] = (q_ref[:, src].astype(F32) * ATTN_SCALE).astype(qo_ref.dtype)
            qo_ref[:, hi] = _extras(ch, ONES3, tm).astype(qo_ref.dtype)
            ko_ref[:, lo] = k_ref[:, src]
            ko_ref[:, hi] = _extras(ONES3, [-p for p in ch], tm).astype(ko_ref.dtype)
            vo_ref[:, lo] = v_ref[:, src]
            vo_ref[:, hi] = _extras(ONES3, ZEROS3, tm).astype(vo_ref.dtype)

    def src(block0):
        return pl.BlockSpec((tm, PACK), lambda i, hp: (i, block0 + hp))

    out = pl.BlockSpec((tm, 2 * PACK), lambda i, hp: (i, hp))
    shp = jax.ShapeDtypeStruct((S, N_FOX_HEADS * PACK), _CD)
    return _pcall(body, name=name, grid=(S // tm, HEAD_PAIRS),
                  in_specs=[src(Q_BLOCK0), src(K_BLOCK0), src(V_BLOCK0), pl.BlockSpec((tm, PACK), lambda i, hp: (i, 0))],
                  out_specs=[out, out, out], out_shape=[shp, shp, shp],
                  compiler_params=_params("parallel", "parallel"))(qkv, qkv, qkv, c)


def _fox_fwd(qp, kp, vp, *, name):
    S = qp.shape[0]
    nt = S // FOX_T
    nt_dims = (((1,), (1,)), ((), ()))
    tn_dims = (((0,), (0,)), ((), ()))

    def body(i_tab, j_tab, q_ref, k_ref, v_ref, o_ref, l_ref, m_s, acc_s):
        t = pl.program_id(1)
        i, j = i_tab[t], j_tab[t]

        @pl.when(j == 0)
        def _():
            m_s[...] = jnp.full((2, 1, FOX_T), NEG_INF, F32)
            acc_s[...] = jnp.zeros((2, PACK, FOX_T), F32)

        def tile(diagonal):
            for hh in range(2):
                cols = slice(hh * PACK, (hh + 1) * PACK)
                st = lax.dot_general(k_ref[:, cols], q_ref[:, cols], nt_dims, preferred_element_type=F32)
                if diagonal:
                    key = lax.broadcasted_iota(jnp.int32, (FOX_T, FOX_T), 0)
                    qry = lax.broadcasted_iota(jnp.int32, (FOX_T, FOX_T), 1)
                    st = jnp.where(key <= qry, st, NEG_INF)
                m_old = m_s[hh]
                m_new = jnp.maximum(m_old, jnp.max(st, axis=0, keepdims=True))
                pt = jnp.exp(st - m_new)
                acc_s[hh] = jnp.exp(m_old - m_new) * acc_s[hh] + lax.dot_general(
                    v_ref[:, cols], pt.astype(_CD), tn_dims, preferred_element_type=F32)
                m_s[hh] = m_new

        @pl.when(j < i)
        def _():
            tile(False)

        @pl.when(j == i)
        def _():
            tile(True)
            for hh in range(2):
                acc = acc_s[hh]
                den = acc[HEAD_DIM:HEAD_DIM + 1, :]
                cols = slice(hh * HEAD_DIM, (hh + 1) * HEAD_DIM)
                o_ref[:, cols] = (acc[:HEAD_DIM, :] / den).T
                l_ref[:, cols] = jnp.broadcast_to(m_s[hh] + jnp.log(den), (HEAD_DIM, FOX_T)).T

    pairs = [(i, j) for i in range(nt) for j in range(i + 1)]
    i_tab = jnp.asarray([p[0] for p in pairs], jnp.int32)
    j_tab = jnp.asarray([p[1] for p in pairs], jnp.int32)
    qs = pl.BlockSpec((FOX_T, 2 * PACK), lambda hp, t, it, jt: (it[t], hp))
    ks = pl.BlockSpec((FOX_T, 2 * PACK), lambda hp, t, it, jt: (jt[t], hp))
    os_ = pl.BlockSpec((FOX_T, PACK), lambda hp, t, it, jt: (it[t], hp))
    shp = jax.ShapeDtypeStruct((S, FOX_WIDTH), F32)
    grid_spec = pltpu.PrefetchScalarGridSpec(
        num_scalar_prefetch=2, grid=(HEAD_PAIRS, len(pairs)), in_specs=[qs, ks, ks], out_specs=[os_, os_],
        scratch_shapes=[pltpu.VMEM((2, 1, FOX_T), F32), pltpu.VMEM((2, PACK, FOX_T), F32)])
    return _pcall(body, name=name, grid_spec=grid_spec, out_shape=[shp, shp],
                  compiler_params=_params("parallel", "arbitrary"))(i_tab, j_tab, qp, kp, vp)


def _fox_pack_bwd(qkv, c, o, lse, do, *, name, tm=512, after=None):
    S = qkv.shape[0]

    def body(q_ref, c_ref, o_ref, l_ref, do_ref, qo_ref, do_out_ref):
        hp = pl.program_id(1)
        cv = c_ref[...]
        for hh in range(2):
            src = slice(hh * HEAD_DIM, (hh + 1) * HEAD_DIM)
            lo = slice(hh * PACK, hh * PACK + HEAD_DIM)
            hi = slice(hh * PACK + HEAD_DIM, (hh + 1) * PACK)
            shift = _head_column(cv, 2 * hp + hh) - l_ref[:, hh * HEAD_DIM:hh * HEAD_DIM + 1]
            dov = do_ref[:, src]
            dsum = jnp.sum(dov * o_ref[:, src], axis=-1, keepdims=True)
            qo_ref[:, lo] = (q_ref[:, src].astype(F32) * ATTN_SCALE).astype(qo_ref.dtype)
            qo_ref[:, hi] = _extras(_pieces(shift), ONES3, tm).astype(qo_ref.dtype)
            do_out_ref[:, lo] = dov.astype(do_out_ref.dtype)
            do_out_ref[:, hi] = _extras(_pieces(-dsum), ZEROS3, tm).astype(do_out_ref.dtype)

    pair = pl.BlockSpec((tm, PACK), lambda i, hp: (i, hp))
    out = pl.BlockSpec((tm, 2 * PACK), lambda i, hp: (i, hp))
    shp = jax.ShapeDtypeStruct((S, N_FOX_HEADS * PACK), _CD)
    return _pcall(body, after, name=name, grid=(S // tm, HEAD_PAIRS),
                  in_specs=[pl.BlockSpec((tm, PACK), lambda i, hp: (i, Q_BLOCK0 + hp)),
                            pl.BlockSpec((tm, PACK), lambda i, hp: (i, 0)), pair, pair, pair],
                  out_specs=[out, out], out_shape=[shp, shp],
                  compiler_params=_params("parallel", "parallel"))(qkv, c, o, lse, do)


def _fox_bwd(qp, kp, vp, dop, *, name):
    S = qp.shape[0]
    nt = S // FOX_T
    nt_dims = (((1,), (1,)), ((), ()))
    tn_dims = (((0,), (0,)), ((), ()))

    def body(i_tab, j_tab, q_ref, k_ref, v_ref, do_ref, dq_ref, dk_ref, dv_ref, dc_ref, dr_ref,
             dq_s, dk_s, dv_s, dc_s, dr_s):
        t = pl.program_id(1)
        i, j = i_tab[t], j_tab[t]

        @pl.when(t == 0)
        def _():
            dq_s[...] = jnp.zeros((S, 2 * PACK), F32)
            dr_s[...] = jnp.zeros((2, 1, S), F32)

        @pl.when(i == j)
        def _():
            dk_s[...] = jnp.zeros((FOX_T, 2 * PACK), F32)
            dv_s[...] = jnp.zeros((FOX_T, 2 * PACK), F32)
            dc_s[...] = jnp.zeros((2, FOX_T, 1), F32)

        def tile(diagonal):
            rows = pl.ds(pl.multiple_of(i * FOX_T, FOX_T), FOX_T)
            for hh in range(2):
                cols = slice(hh * PACK, (hh + 1) * PACK)
                qv, kv, vv, dov = q_ref[:, cols], k_ref[:, cols], v_ref[:, cols], do_ref[:, cols]
                pt = jnp.exp(lax.dot_general(kv, qv, nt_dims, preferred_element_type=F32))
                if diagonal:
                    key = lax.broadcasted_iota(jnp.int32, (FOX_T, FOX_T), 0)
                    qry = lax.broadcasted_iota(jnp.int32, (FOX_T, FOX_T), 1)
                    pt = jnp.where(key <= qry, pt, 0.0)
                dst = pt * lax.dot_general(vv, dov, nt_dims, preferred_element_type=F32)
                dsb = dst.astype(_CD)
                dc_s[hh] += jnp.sum(dst, axis=1, keepdims=True)
                dr_s[hh, :, rows] += jnp.sum(dst, axis=0, keepdims=True)
                dv_s[:, cols] += jnp.dot(pt.astype(_CD), dov, preferred_element_type=F32)
                dk_s[:, cols] += jnp.dot(dsb, qv, preferred_element_type=F32)
                dq_s[rows, cols] += lax.dot_general(dsb, kv, tn_dims, preferred_element_type=F32)

        @pl.when(i > j)
        def _():
            tile(False)

        @pl.when(i == j)
        def _():
            tile(True)

        @pl.when(i == nt - 1)
        def _():
            for hh in range(2):
                src = slice(hh * PACK, hh * PACK + HEAD_DIM)
                dst_cols = slice(hh * HEAD_DIM, (hh + 1) * HEAD_DIM)
                dk_ref[:, dst_cols] = dk_s[:, src].astype(dk_ref.dtype)
                dv_ref[:, dst_cols] = dv_s[:, src].astype(dv_ref.dtype)
                dc_ref[:, dst_cols] = jnp.broadcast_to(dc_s[hh], (FOX_T, HEAD_DIM))

        @pl.when(t == len(pairs) - 1)
        def _():
            for hh in range(2):
                dq_ref[:, hh * HEAD_DIM:(hh + 1) * HEAD_DIM] = (
                    dq_s[:, hh * PACK:hh * PACK + HEAD_DIM] * ATTN_SCALE).astype(dq_ref.dtype)
            dr_ref[...] = dr_s[...]

    pairs = [(i, j) for j in range(nt) for i in range(j, nt)]
    i_tab = jnp.asarray([p[0] for p in pairs], jnp.int32)
    j_tab = jnp.asarray([p[1] for p in pairs], jnp.int32)
    qs = pl.BlockSpec((FOX_T, 2 * PACK), lambda hp, t, it, jt: (it[t], hp))
    ks = pl.BlockSpec((FOX_T, 2 * PACK), lambda hp, t, it, jt: (jt[t], hp))
    whole = pl.BlockSpec((S, PACK), lambda hp, t, it, jt: (0, hp))
    cs = pl.BlockSpec((FOX_T, PACK), lambda hp, t, it, jt: (jt[t], hp))
    rs = pl.BlockSpec((2, 1, S), lambda hp, t, it, jt: (hp, 0, 0))
    shp = jax.ShapeDtypeStruct((S, FOX_WIDTH), _CD)
    grid_spec = pltpu.PrefetchScalarGridSpec(
        num_scalar_prefetch=2, grid=(HEAD_PAIRS, len(pairs)), in_specs=[qs, ks, ks, qs],
        out_specs=[whole, cs, cs, cs, rs],
        scratch_shapes=[pltpu.VMEM((S, 2 * PACK), F32), pltpu.VMEM((FOX_T, 2 * PACK), F32),
                        pltpu.VMEM((FOX_T, 2 * PACK), F32), pltpu.VMEM((2, FOX_T, 1), F32),
                        pltpu.VMEM((2, 1, S), F32)])
    return _pcall(body, name=name, grid_spec=grid_spec,
                  out_shape=[shp, shp, shp, jax.ShapeDtypeStruct((S, FOX_WIDTH), F32),
                             jax.ShapeDtypeStruct((N_FOX_HEADS, 1, S), F32)],
                  compiler_params=_params("parallel", "arbitrary"))(i_tab, j_tab, qp, kp, vp, dop)


def _dedilate(t, d):
    if d == 1:
        return t
    S, C = t.shape
    return t.reshape(S // d, d, C).transpose(1, 0, 2).reshape(S, C)


def _redilate(t, d):
    if d == 1:
        return t
    S, C = t.shape
    return t.reshape(d, S // d, C).transpose(1, 0, 2).reshape(S, C)


def _layer_step(x, tgt, w, p, late_weights=None, grad_sink=None, after=None, first_weights=None):
    S = x.shape[0]
    h = _rms_fwd(x, p["norm_mix_g"], name="rms_mix")
    if first_weights is not None:
        w = {**w, **first_weights(h)}
    qkv = _mm(h, w["qkv"], name="proj_qkv", out_dtype=_CD, tn=768, tm=2048, after=after)
    zf = _mm(h, w["f"], name="proj_f")
    gl = _mm(h, w["g"], name="proj_gate", tn=1024)

    dil_q, dil_k, dil_v = [], [], []
    dil_o, dil_l = [], []
    for g, (_, d) in enumerate(DIL_PAIRS):
        qg = _dedilate(qkv[:, g * DIL_OUT:(g + 1) * DIL_OUT], d)
        kg = _dedilate(qkv[:, DIL_WIDTH + g * DIL_OUT:DIL_WIDTH + (g + 1) * DIL_OUT], d)
        vg = _dedilate(qkv[:, 2 * DIL_WIDTH + g * DIL_OUT:2 * DIL_WIDTH + (g + 1) * DIL_OUT], d)
        og, lg = _dil_fwd(qg, kg, vg, g, name=f"dil_fwd{g}")
        dil_q.append(qg), dil_k.append(kg), dil_v.append(vg)
        dil_o.append(_redilate(og, d)), dil_l.append(_redilate(lg, d))
    o_a = _dil_mix_fwd(dil_o, dil_l, name="dil_mix")

    c = _fox_cumsum(zf, p["b_fgt"], name="fox_cumsum")
    fqp, fkp, fvp = _fox_pack_fwd(qkv, c, name="fox_pack")
    o_b, flse = _fox_fwd(fqp, fkp, fvp, name="fox_fwd")

    if late_weights is not None:
        w = {**w, **late_weights(o_b)}
    y_a = _mm(o_a, w["dil_out"], name="y_a", tn=1024)
    y_b = _mm(o_b, w["fox_out"], name="y_b", tn=1024)
    merged = _gate_fwd(gl, p["b_gate"], y_a, y_b, name="gate_fwd")
    x1 = _mm(merged, w["out"], name="mix_out", add=x)

    h2 = _rms_fwd(x1, p["norm_ffn_g"], name="rms_ffn")
    gu = _mm(h2, w["ffn_in"], name="ffn_in", tn=1408, b_blocks=True, out_dtype=_CD, tm=2048)
    act = _swiglu_fwd(gu, name="swiglu")
    x2 = _mm(act, w["ffn_down"], name="ffn_down", add=x1, tk=2816)

    loss, dx2, dg_final = _loss_head(x2, p["norm_final_g"], tgt, name="loss_head")

    dact = _mm(dx2, w["ffn_down"], name="d_act", tb=True, tn=1408, out_dtype=_CD)
    gw_ffn_down = _mm(act, dx2, name="gw_ffn_down", ta=True, out_dtype=_CD, tm=1408)
    dgu = _swiglu_bwd(dact, gu, name="swiglu_bwd")
    dh2 = _mm(dgu, w["ffn_in"], name="d_h2", tb=True, tk=1408, b_blocks=True, tm=2048)
    gw_ffn_in = _mm(h2, dgu, name="gw_ffn_in", ta=True, out_dtype=_CD, tn=1408, out_blocks=1408)
    sink = grad_sink if grad_sink is not None else (lambda group, grads: None)
    tok = sink("ffn", dict(ffn_in=gw_ffn_in, ffn_down=gw_ffn_down))
    dx1, dg_ffn = _rms_bwd(x1, p["norm_ffn_g"], dh2, dx2, name="rms_ffn_bwd", after=tok)

    dmerged = _mm(dx1, w["out"], name="d_merged", tb=True)
    gw_out = _mm(merged, dx1, name="gw_out", ta=True, out_dtype=_CD)
    dy_a, dy_b, dgl, db_gate = _gate_bwd(dmerged, gl, p["b_gate"], y_a, y_b, name="gate_bwd")
    do_a = _mm(dy_a, w["dil_out"], name="d_o_a", tb=True)
    gw_dil_out = _mm(o_a, dy_a, name="gw_dil_out", ta=True, out_dtype=_CD, tn=1024)
    do_b = _mm(dy_b, w["fox_out"], name="d_o_b", tb=True)
    gw_fox_out = _mm(o_b, dy_b, name="gw_fox_out", ta=True, out_dtype=_CD, tn=1024)
    tok = sink("mix", dict(dil_out=gw_dil_out, fox_out=gw_fox_out, out=gw_out))

    bqp, bdop = _fox_pack_bwd(qkv, c, o_b, flse, do_b, name="fox_pack_bwd", after=tok)
    dqp, dkp, dvp, dck, dcq = _fox_bwd(bqp, fkp, fvp, bdop, name="fox_bwd")
    dc = dcq[:, 0, :].T - dck.reshape(S, N_FOX_HEADS, HEAD_DIM)[:, :, 0]
    dc = jnp.pad(dc, ((0, 0), (0, F_PAD - N_FOX_HEADS)))
    dzf, db_fgt = _fox_cumsum_bwd(dc, zf, p["b_fgt"], name="fox_cumsum_bwd")

    douts = _dil_mix_bwd(do_a, dil_o, dil_l, name="dil_mix_bwd", after=tok)
    dqs, dks, dvs = [], [], []
    for g, (_, d) in enumerate(DIL_PAIRS):
        dq, dk, dv = _dil_bwd(dil_q[g], dil_k[g], dil_v[g], _dedilate(dil_o[g], d), _dedilate(dil_l[g], d),
                              _dedilate(douts[g], d), _dedilate(douts[3 + g], d), g, name=f"dil_bwd{g}")
        dqs.append(_redilate(dq, d)), dks.append(_redilate(dk, d)), dvs.append(_redilate(dv, d))
    dqkv = jnp.concatenate(dqs + dks + dvs + [dqp, dkp, dvp], axis=1)

    gw_qkv = _mm(h, dqkv, name="gw_qkv", ta=True, out_dtype=_CD, tn=768)
    gw_g = _mm(h, dgl, name="gw_gate", ta=True, out_dtype=_CD)
    gw_f = _mm(h, dzf, name="gw_f", ta=True, out_dtype=_CD)
    tok = sink("in", dict(qkv=gw_qkv, f=gw_f, g=gw_g))
    dh = _mm(dqkv, w["qkv"], name="d_h_qkv", tb=True, tk=1920, tm=2048, after=tok)
    dh = _mm(dgl, w["g"], name="d_h_gate", tb=True, add=dh)
    dh = _mm(dzf, w["f"], name="d_h_f", tb=True, add=dh)
    dx, dg_mix = _rms_bwd(x, p["norm_mix_g"], dh, dx1, name="rms_mix_bwd")

    gw = dict(qkv=gw_qkv, f=gw_f, g=gw_g, dil_out=gw_dil_out, fox_out=gw_fox_out, out=gw_out, ffn_in=gw_ffn_in,
              ffn_down=gw_ffn_down)
    small = dict(norm_mix_g=dg_mix, b_fgt=db_fgt, b_gate=db_gate, norm_ffn_g=dg_ffn, norm_final_g=dg_final)
    return loss, dx, gw, small


def _position():
    return lax.axis_index("x"), lax.axis_index("y"), lax.axis_index("c")


def _other_chips(x, y):
    return [(1 - x, y), (x, 1 - y), (1 - x, 1 - y)]


ROW_TILE = 16


def _row_chunks(rows, want=4):
    n = want
    while n > 1 and rows % (n * ROW_TILE):
        n //= 2
    return n


SEM_SPEC = pl.BlockSpec(memory_space=pltpu.SEMAPHORE)
ANY_SPEC = pl.BlockSpec(memory_space=pl.ANY)
DATAFLOW = pltpu.SideEffectType.DATAFLOW_SIDE_EFFECTING


def _in_hbm(a):
    return pltpu.with_memory_space_constraint(a, pltpu.HBM)


def _split_copy_start(srcs, land_shapes, copies, after, *, name):
    n, m = len(srcs), len(land_shapes)

    def body(*refs):
        src_refs, land_refs = refs[:n], refs[n:n + m]
        send_sems, recv_sems = refs[n + m + 1], refs[n + m + 2]
        token = refs[-1]
        x, y, c = _position()
        for k, (src, dst, peer) in enumerate(copies(x, y, c, src_refs, land_refs)):
            pltpu.make_async_remote_copy(src_ref=src, dst_ref=dst, send_sem=send_sems.at[k], recv_sem=recv_sems.at[k],
                                         device_id=peer, device_id_type=MESH).start()
        token[...] = jnp.zeros_like(token)

    lands = [lax.empty(s.shape, s.dtype) for s in land_shapes]
    count = len(copies(0, 0, 0, srcs, lands))
    out = _pcall(
        body, name=name,
        out_shape=(pltpu.SemaphoreType.DMA((count,)), pltpu.SemaphoreType.DMA((count,)),
                   *[pltpu.HBM(s.shape, s.dtype) for s in srcs], *[pltpu.HBM(s.shape, s.dtype) for s in land_shapes],
                   jax.ShapeDtypeStruct((8, 128), F32)),
        in_specs=[HBM_SPEC] * (n + m) + [ANY_SPEC],
        out_specs=(SEM_SPEC, SEM_SPEC, *[HBM_SPEC] * (n + m), pl.BlockSpec(memory_space=pltpu.VMEM)),
        input_output_aliases={k: 2 + k for k in range(n + m)},
        compiler_params=pltpu.CompilerParams(has_side_effects=DATAFLOW),
    )(*[_in_hbm(s) for s in srcs], *[_in_hbm(l) for l in lands], after)
    return out[0], out[1], list(out[2:2 + n]), list(out[2 + n:2 + n + m]), out[-1]


def _split_copy_wait(send_sems, recv_sems, srcs, lands, copies, after, *, name):
    n, m = len(srcs), len(lands)

    def body(*refs):
        src_refs, land_refs = refs[:n], refs[n:n + m]
        send, recv = refs[n + m], refs[n + m + 1]
        x, y, c = _position()
        for k, (src, dst, peer) in enumerate(copies(x, y, c, src_refs, land_refs)):
            cp = pltpu.make_async_remote_copy(src_ref=src, dst_ref=dst, send_sem=send.at[k], recv_sem=recv.at[k],
                                              device_id=peer, device_id_type=MESH)
            cp.wait_send()
            cp.wait_recv()

    afters = list(after) if isinstance(after, (list, tuple)) else [after]
    out = _pcall(
        body, name=name,
        out_shape=tuple(pltpu.HBM(s.shape, s.dtype) for s in list(srcs) + list(lands)),
        in_specs=[HBM_SPEC] * (n + m) + [SEM_SPEC, SEM_SPEC] + [ANY_SPEC] * len(afters),
        out_specs=tuple([HBM_SPEC] * (n + m)),
        input_output_aliases={k: k for k in range(n + m)},
        compiler_params=pltpu.CompilerParams(has_side_effects=DATAFLOW),
    )(*srcs, *lands, send_sems, recv_sems, *afters)
    return list(out[:n]), list(out[n:])


def _gather_copies(x, y, c, shard_refs, land_refs):
    out = []
    for s, l in zip(shard_refs, land_refs):
        half = s.shape[0] // 2
        nq = _row_chunks(half)
        for cx, cy in _other_chips(x, y):
            for q in range(nq):
                rows = pl.ds(c * half + q * (half // nq), half // nq)
                out.append((s.at[rows, :], l.at[2 * x + y, rows, :], (cx, cy, c)))
    return out


def _scatter_copies(x, y, c, part_refs, land_refs):
    out = []
    for p, l in zip(part_refs, land_refs):
        nq = _row_chunks(p.shape[1])
        for r, (cx, cy) in enumerate(_other_chips(x, y)):
            for q in range(nq):
                rows = pl.ds(q * (p.shape[1] // nq), p.shape[1] // nq)
                out.append((p.at[2 * cx + cy, rows, :], l.at[r, rows, :], (cx, cy, c)))
    return out


def _forward_halves(lands, *, name):
    n = len(lands)

    def body(*refs):
        ins = refs[:n]
        send_sems, recv_sems = refs[2 * n:]
        x, y, c = _position()
        copies = []
        for w in range(n):
            half = ins[w].shape[1] // 2
            for r, (cx, cy) in enumerate(_other_chips(x, y)):
                blk = ins[w].at[2 * cx + cy, pl.ds(c * half, half), :]
                cp = pltpu.make_async_remote_copy(src_ref=blk, dst_ref=blk, send_sem=send_sems.at[w, r],
                                                  recv_sem=recv_sems.at[w, r], device_id=(x, y, 1 - c),
                                                  device_id_type=MESH)
                cp.start()
                copies.append(cp)
        for w in range(n):
            half = ins[w].shape[1] // 2
            for r, (cx, cy) in enumerate(_other_chips(x, y)):
                blk = ins[w].at[2 * cx + cy, pl.ds((1 - c) * half, half), :]
                pltpu.make_async_remote_copy(src_ref=blk, dst_ref=blk, send_sem=send_sems.at[w, r],
                                             recv_sem=recv_sems.at[w, r], device_id=(x, y, 1 - c),
                                             device_id_type=MESH).wait_recv()
        for cp in copies:
            cp.wait_send()

    return _pcall(
        body, name=name, in_specs=[HBM_SPEC] * n, out_specs=[HBM_SPEC] * n,
        out_shape=[jax.ShapeDtypeStruct(l.shape, l.dtype) for l in lands],
        input_output_aliases={k: k for k in range(n)},
        scratch_shapes=[pltpu.SemaphoreType.DMA((n, 3)), pltpu.SemaphoreType.DMA((n, 3))],
    )(*lands)


def _swap_halves(grads, name="swap_halves"):
    n = len(grads)

    def body(*refs):
        ins, outs = refs[:n], refs[n:2 * n]
        send_sems, recv_sems = refs[2 * n:]
        x, y, c = _position()
        copies = []
        for w in range(n):
            half = ins[w].shape[1] // 2
            cp = pltpu.make_async_remote_copy(
                src_ref=ins[w].at[:, pl.ds((1 - c) * half, half), :], dst_ref=outs[w], send_sem=send_sems.at[w],
                recv_sem=recv_sems.at[w], device_id=(x, y, 1 - c), device_id_type=MESH)
            cp.start()
            copies.append(cp)
        for cp in copies:
            cp.wait()

    return _pcall(
        body, name=name, in_specs=[HBM_SPEC] * n, out_specs=[HBM_SPEC] * n,
        out_shape=[jax.ShapeDtypeStruct((4, g.shape[1] // 2, g.shape[2]), g.dtype) for g in grads],
        scratch_shapes=[pltpu.SemaphoreType.DMA((n,)), pltpu.SemaphoreType.DMA((n,))],
    )(*grads)


def _share_halves(halves):
    n = len(halves)

    def body(*refs):
        ins, outs = refs[:n], refs[n:2 * n]
        send_sems, recv_sems = refs[2 * n:]
        x, y, c = _position()
        copies = []
        for w in range(n):
            cp = pltpu.make_async_remote_copy(src_ref=ins[w], dst_ref=outs[w], send_sem=send_sems.at[w],
                                              recv_sem=recv_sems.at[w], device_id=(x, y, 1 - c), device_id_type=MESH)
            cp.start()
            copies.append(cp)
        for cp in copies:
            cp.wait()

    return _pcall(
        body, name="share_halves", in_specs=[HBM_SPEC] * n, out_specs=[HBM_SPEC] * n,
        out_shape=[jax.ShapeDtypeStruct(h.shape, h.dtype) for h in halves],
        scratch_shapes=[pltpu.SemaphoreType.DMA((n,)), pltpu.SemaphoreType.DMA((n,))],
    )(*halves)


def _sum_small(part):
    rows, width = part.shape

    def body(x_ref, out_ref, all_ref, send_sems, recv_sems):
        x, y, c = _position()
        me, sibling = (x, y, c), (x, y, 1 - c)
        chips = _other_chips(x, y)

        def block(px, py, pc):
            return all_ref.at[pl.ds((4 * px + 2 * py + pc) * rows, rows), :]

        def copy(k, blk, to, src=None):
            return pltpu.make_async_remote_copy(
                src_ref=block(*blk) if src is None else src, dst_ref=block(*blk), send_sem=send_sems.at[k],
                recv_sem=recv_sems.at[k], device_id=to, device_id_type=MESH)

        all_ref[pl.ds((4 * x + 2 * y + c) * rows, rows), :] = x_ref[...]
        first = [copy(0, me, sibling, src=x_ref)]
        first += [copy(1 + j, me, (*chip, c), src=x_ref) for j, chip in enumerate(chips)]
        for cp in first:
            cp.start()
        passed = [copy(4 + j, (*chip, c), sibling) for j, chip in enumerate(chips)]
        for j, chip in enumerate(chips):
            copy(1 + j, (*chip, c), me).wait_recv()
            passed[j].start()
        copy(0, sibling, me).wait_recv()
        for j, chip in enumerate(chips):
            copy(4 + j, (*chip, 1 - c), me).wait_recv()
        for cp in first + passed:
            cp.wait_send()
        total = all_ref[0:rows, :]
        for d in range(1, 8):
            total = total + all_ref[d * rows:(d + 1) * rows, :]
        out_ref[...] = total

    vm = pl.BlockSpec(memory_space=pltpu.VMEM)
    return _pcall(
        body, name="sum_small", in_specs=[vm], out_specs=vm, out_shape=jax.ShapeDtypeStruct((rows, width), F32),
        scratch_shapes=[pltpu.VMEM((8 * rows, width), F32), pltpu.SemaphoreType.DMA((7,)), pltpu.SemaphoreType.DMA((7,))],
    )(part)


def _row_tile(R, C, itemsize=4, budget=1 << 20):
    for t in (512, 256, 128, 64, 32, 16, 8):
        if R % t == 0 and t * C * itemsize <= budget:
            return t
    return R


def _add_halves(g, recv, c, *, name):
    _, R, C = g.shape
    half = R // 2
    t = _row_tile(half, C)
    nb = half // t

    def body(c_ref, g_ref, r_ref, o_ref):
        o_ref[...] = (g_ref[...].astype(F32) + r_ref[...].astype(F32)).astype(o_ref.dtype)

    grid_spec = pltpu.PrefetchScalarGridSpec(
        num_scalar_prefetch=1, grid=(4, nb),
        in_specs=[pl.BlockSpec((1, t, C), lambda k, i, cr: (k, cr[0] * nb + i, 0)),
                  pl.BlockSpec((1, t, C), lambda k, i, cr: (k, i, 0))],
        out_specs=pl.BlockSpec((1, t, C), lambda k, i, cr: (k, i, 0)))
    return _pcall(body, name=name, grid_spec=grid_spec, out_shape=jax.ShapeDtypeStruct((4, half, C), g.dtype),
                  compiler_params=_params("parallel", "parallel"))(c, g, recv)


def _add_owners(mine, recv, *, name):
    half, C = mine.shape
    t = _row_tile(half, C)

    def body(m_ref, r_ref, o_ref):
        o_ref[...] = ((m_ref[...].astype(F32) + r_ref[0].astype(F32)) + r_ref[1].astype(F32)) + r_ref[2].astype(F32)

    return _pcall(body, name=name, grid=(half // t,),
                  in_specs=[pl.BlockSpec((t, C), lambda i: (i, 0)), pl.BlockSpec((3, t, C), lambda i: (0, i, 0))],
                  out_specs=pl.BlockSpec((t, C), lambda i: (i, 0)), out_shape=jax.ShapeDtypeStruct((half, C), F32),
                  compiler_params=_params("parallel"))(mine, recv)


def _adamw(w, g, m, v, *, name):
    R, C = w.shape
    t = _row_tile(R, C)
    c1 = 1.0 - ADAM_B1 ** ADAM_STEP
    c2 = 1.0 - ADAM_B2 ** ADAM_STEP

    def body(w_ref, g_ref, m_ref, v_ref, d_ref, nm_ref, nv_ref):
        gv = g_ref[...]
        mn = ADAM_B1 * m_ref[...] + (1.0 - ADAM_B1) * gv
        vn = ADAM_B2 * v_ref[...] + (1.0 - ADAM_B2) * (gv * gv)
        d_ref[...] = -ADAM_LR * ((mn / c1) / (jnp.sqrt(vn / c2) + ADAM_EPS) + ADAM_WD * w_ref[...])
        nm_ref[...] = mn
        nv_ref[...] = vn

    blk = pl.BlockSpec((t, C), lambda i: (i, 0))
    shp = jax.ShapeDtypeStruct((R, C), F32)
    return _pcall(body, name=name, grid=(R // t,), in_specs=[blk] * 4, out_specs=[blk] * 3, out_shape=[shp] * 3,
                  compiler_params=_params("parallel"))(w, g, m, v)


BIG = ("w_in", "w_dil_out", "w_fox_out", "w_out", "w_ffn_in", "w_ffn_down")
SMALL = ("norm_mix_g", "b_fgt", "b_gate", "norm_ffn_g", "norm_final_g")
ORDER = ("norm_mix_g", "w_in", "b_fgt", "b_gate", "w_dil_out", "w_fox_out", "w_out", "norm_ffn_g", "w_ffn_in",
         "w_ffn_down", "norm_final_g")
SMALL_ROWS = {"norm_mix_g": (0, 1), "b_gate": (1, 3), "norm_ffn_g": (3, 4), "norm_final_g": (4, 5), "b_fgt": (5, 6)}


def _columns_to_blocks(full, ncol):
    K = full.shape[0]
    return full.reshape(K, 4, ncol).transpose(1, 0, 2)


def _blocks_to_columns(blocks):
    n, K, ncol = blocks.shape
    return blocks.transpose(1, 0, 2).reshape(K, n * ncol)


def kernel(x, norm_mix_g, w_in, b_fgt, b_gate, w_dil_out, w_fox_out, w_out, norm_ffn_g, w_ffn_in, w_ffn_down, norm_final_g, loss_target, m_norm_mix_g, m_w_in, m_b_fgt, m_b_gate, m_w_dil_out, m_w_fox_out, m_w_out, m_norm_ffn_g, m_w_ffn_in, m_w_ffn_down, m_norm_final_g, v_norm_mix_g, v_w_in, v_b_fgt, v_b_gate, v_w_dil_out, v_w_fox_out, v_w_out, v_norm_ffn_g, v_w_ffn_in, v_w_ffn_down, v_norm_final_g):
    weights = dict(norm_mix_g=norm_mix_g, w_in=w_in, b_fgt=b_fgt, b_gate=b_gate, w_dil_out=w_dil_out,
                   w_fox_out=w_fox_out, w_out=w_out, norm_ffn_g=norm_ffn_g, w_ffn_in=w_ffn_in, w_ffn_down=w_ffn_down,
                   norm_final_g=norm_final_g)
    m_in = dict(norm_mix_g=m_norm_mix_g, w_in=m_w_in, b_fgt=m_b_fgt, b_gate=m_b_gate, w_dil_out=m_w_dil_out,
                w_fox_out=m_w_fox_out, w_out=m_w_out, norm_ffn_g=m_norm_ffn_g, w_ffn_in=m_w_ffn_in,
                w_ffn_down=m_w_ffn_down, norm_final_g=m_norm_final_g)
    v_in = dict(norm_mix_g=v_norm_mix_g, w_in=v_w_in, b_fgt=v_b_fgt, b_gate=v_b_gate, w_dil_out=v_w_dil_out,
                w_fox_out=v_w_fox_out, w_out=v_w_out, norm_ffn_g=v_norm_ffn_g, w_ffn_in=v_w_ffn_in,
                w_ffn_down=v_w_ffn_down, norm_final_g=v_norm_final_g)
    c = lax.axis_index("c")
    chip = 2 * lax.axis_index("x") + lax.axis_index("y")

    shards = {n: weights[n][0].astype(_CD) for n in BIG}
    in_shape = jax.ShapeDtypeStruct((4,) + shards["w_in"].shape, _CD)
    send_i, recv_i, in_src, in_land, token_in = _split_copy_start(
        [shards["w_in"]], [in_shape], _gather_copies, norm_mix_g, name="gather_in_start")
    late = BIG[1:]
    send_g, recv_g, late_src, late_land, token = _split_copy_start(
        [shards[n] for n in late], [jax.ShapeDtypeStruct((4,) + shards[n].shape, _CD) for n in late],
        _gather_copies, token_in, name="gather_late_start")
    adam_in = [t[0] for t in (w_in, m_w_in, v_w_in)]
    p = dict(norm_mix_g=norm_mix_g, b_fgt=jnp.pad(b_fgt, ((0, 0), (0, F_PAD - N_FOX_HEADS))), b_gate=b_gate,
             norm_ffn_g=norm_ffn_g, norm_final_g=norm_final_g.reshape(1, D_MODEL))

    def first_weights(after):
        own, lands = _split_copy_wait(send_i, recv_i, in_src, in_land, _gather_copies, [after] + adam_in,
                                      name="gather_in_wait")
        (g_in,) = _forward_halves(lands, name="gather_in_forward")
        full_in = _blocks_to_columns(lax.dynamic_update_index_in_dim(g_in, own[0], chip, 0))
        o3 = QKV_COLS
        o4 = o3 + N_FOX_HEADS
        return dict(qkv=full_in[:, :o3], f=jnp.pad(full_in[:, o3:o4], ((0, 0), (0, F_PAD - N_FOX_HEADS))),
                    g=full_in[:, o4:])

    def late_weights(after):
        own, lands = _split_copy_wait(send_g, recv_g, late_src, late_land, _gather_copies, after,
                                      name="gather_late_wait")
        lands = _forward_halves(lands, name="gather_late_forward")
        g_dil, g_fox, g_out, g_ffn_in, g_ffn_down = [
            lax.dynamic_update_index_in_dim(l, s, chip, 0) for l, s in zip(lands, own)]
        return dict(dil_out=_blocks_to_columns(g_dil), fox_out=_blocks_to_columns(g_fox),
                    out=g_out.reshape(D_MODEL, D_MODEL), ffn_in=g_ffn_in,
                    ffn_down=g_ffn_down.reshape(D_FF, D_MODEL))

    c_arr = jnp.reshape(c, (1,)).astype(jnp.int32)

    def to_blocks(n, full):
        shape = weights[n].shape
        if full.ndim == 3:
            return full
        if n in ("w_out", "w_ffn_down"):
            return full.reshape(4, shape[1], shape[2])
        return _columns_to_blocks(full, shape[2])

    def pair_sums(group, named):
        names = list(named)
        blocks = [to_blocks(n, named[n]) for n in names]
        from_sibling = _swap_halves(blocks, name=f"swap_halves_{group}")
        return [_add_halves(b, r, c_arr, name=f"add_halves_{n}") for b, r, n in zip(blocks, from_sibling, names)]

    in_flight = {}

    def grad_sink(group, gw):
        if group == "in":
            named = {"w_in": jnp.concatenate([gw["qkv"], gw["f"][:, :N_FOX_HEADS], gw["g"]], axis=1)}
        else:
            named = {"w_" + k: v for k, v in gw.items()}
        sums = pair_sums(group, named)
        started = _split_copy_start(sums, [jax.ShapeDtypeStruct((3,) + s.shape[1:], s.dtype) for s in sums],
                                    _scatter_copies, next(iter(gw.values())), name=f"scatter_{group}_start")
        in_flight[group] = (list(named), started)
        return started[-1]

    loss_part, grad_x, gw, small = _layer_step(x[0], loss_target[0], {}, p, late_weights, grad_sink, token,
                                               first_weights)

    def owner_sums(names, sums, from_chips):
        return {n: _add_owners(lax.dynamic_index_in_dim(s, chip, 0, keepdims=False), r, name=f"add_owners_{n}")
                for n, s, r in zip(names, sums, from_chips)}

    halves = {}
    for group, (names, (send_s, recv_s, srcs, lands, _)) in in_flight.items():
        sums, from_chips = _split_copy_wait(send_s, recv_s, srcs, lands, _scatter_copies, grad_x,
                                            name=f"scatter_{group}_wait")
        halves.update(owner_sums(names, sums, from_chips))
    halves = [halves[n] for n in BIG]
    grads = {}
    for n, own, other in zip(BIG, halves, _share_halves(halves)):
        pair = jnp.stack([own, other])
        grads[n] = jnp.where(c == 0, pair, pair[::-1]).reshape(2 * own.shape[0], own.shape[1])

    packed = jnp.concatenate([
        small["norm_mix_g"], small["b_gate"].reshape(2, D_MODEL), small["norm_ffn_g"], small["norm_final_g"],
        jnp.pad(small["b_fgt"], ((0, 0), (0, D_MODEL - F_PAD))), jnp.zeros((2, D_MODEL), F32)], axis=0)
    summed = _sum_small(packed)
    for n in SMALL:
        lo, hi = SMALL_ROWS[n]
        grads[n] = summed[lo:hi].reshape(1, -1)[:, :weights[n].size]

    loss = lax.psum(loss_part[0, 0], ("x", "y", "c"))

    out_g, out_d, out_m, out_v = {}, {}, {}, {}
    for n in ORDER:
        shape = weights[n].shape
        two_d = shape[1:] if len(shape) == 3 else (1, weights[n].size)
        g2 = grads[n].reshape(two_d)
        wmv = adam_in if n == "w_in" else [t.reshape(two_d) for t in (weights[n], m_in[n], v_in[n])]
        d2, m2, v2 = _adamw(wmv[0], g2, wmv[1], wmv[2], name=f"adamw_{n}")
        out_g[n], out_d[n], out_m[n], out_v[n] = (g2.reshape(shape), d2.reshape(shape), m2.reshape(shape),
                                                  v2.reshape(shape))
    return (loss, grad_x[None], *[out_g[n] for n in ORDER], *[out_d[n] for n in ORDER],
            *[out_m[n] for n in ORDER], *[out_v[n] for n in ORDER])
```

```python
import numpy as np
import jax
import jax.numpy as jnp
from jax import lax
from jax.experimental import pallas as pl
from jax.experimental.pallas import tpu as pltpu

F32 = jnp.float32
_CD = jnp.bfloat16

D_MODEL = 1024
HEAD_DIM = 64
DIL_PAIRS = ((128, 1), (512, 4), (2048, 16))
N_DIL_GROUPS = 3
DIL_HEADS = 4
DIL_W = 128
DIL_OUT = DIL_HEADS * HEAD_DIM
DIL_WIDTH = N_DIL_GROUPS * DIL_OUT
N_FOX_HEADS = 8
FOX_WIDTH = N_FOX_HEADS * HEAD_DIM
D_FF = 2816
QKV_COLS = 3 * DIL_WIDTH + 3 * FOX_WIDTH
F_PAD = 128
RMS_EPS = 1e-6
NEG_INF = -1e30
ATTN_SCALE = HEAD_DIM ** -0.5
ADAM_LR, ADAM_B1, ADAM_B2, ADAM_EPS, ADAM_WD, ADAM_STEP = 0.001, 0.9, 0.999, 1e-08, 0.01, 10

VMEM_LIMIT = 48 * 1024 * 1024
MESH = pl.DeviceIdType.MESH
HBM_SPEC = pl.BlockSpec(memory_space=pltpu.HBM)


def _pcall(body, after=None, **kw):
    if after is None:
        return pl.pallas_call(body, **kw)
    n_in = len(kw["in_specs"])
    kw["in_specs"] = list(kw["in_specs"]) + [pl.BlockSpec(memory_space=pl.ANY)]

    def tied(*refs):
        return body(*refs[:n_in], *refs[n_in + 1:])

    call = pl.pallas_call(tied, **kw)
    return lambda *args: call(*args, after)


def _params(*sem):
    return pltpu.CompilerParams(dimension_semantics=sem, vmem_limit_bytes=VMEM_LIMIT)


def _pick(dim, pref):
    t = (min(pref, dim) // 128) * 128
    while t >= 128:
        if dim % t == 0:
            return t
        t -= 128
    return dim


def _mm(a, b, *, name, ta=False, tb=False, out_dtype=F32, add=None, tm=1024, tn=512, tk=2048, after=None,
        b_blocks=False, out_blocks=None):
    if ta:
        K, M = a.shape
    else:
        M, K = a.shape
    b_rows, b_cols = (b.shape[1], b.shape[0] * b.shape[2]) if b_blocks else b.shape
    if tb:
        N, K2 = b_rows, b_cols
    else:
        K2, N = b_rows, b_cols
    assert K == K2, (a.shape, b.shape)
    shard = b.shape[2] if b_blocks else None
    tm = _pick(M, tm)
    tn = _pick(shard if (b_blocks and not tb) else (out_blocks or N), tn)
    tk = _pick(shard if (b_blocks and tb) else K, tk)
    nk = K // tk
    dn = (((0 if ta else 1,), (1 if tb else 0,)), ((), ()))
    has_add = add is not None
    assert not (has_add and out_blocks)

    def body(*refs):
        a_ref, b_ref = refs[0], refs[1]
        add_ref = refs[2] if has_add else None
        o_ref = refs[3] if has_add else refs[2]
        bv = b_ref[0] if b_blocks else b_ref[...]
        p = lax.dot_general(a_ref[...].astype(_CD), bv.astype(_CD), dn, preferred_element_type=F32)

        def finish(r):
            if has_add:
                r = r + add_ref[...]
            if out_blocks:
                o_ref[0] = r.astype(out_dtype)
            else:
                o_ref[...] = r.astype(out_dtype)

        if nk == 1:
            finish(p)
        else:
            acc_ref = refs[-1]
            k = pl.program_id(2)

            @pl.when(k == 0)
            def _():
                acc_ref[...] = p

            @pl.when(k > 0)
            def _():
                acc_ref[...] += p

            @pl.when(k == nk - 1)
            def _():
                finish(acc_ref[...])

    a_spec = pl.BlockSpec((tk, tm), lambda i, j, k: (k, i)) if ta else pl.BlockSpec((tm, tk), lambda i, j, k: (i, k))
    if b_blocks and tb:
        per = shard // tk
        b_spec = pl.BlockSpec((1, tn, tk), lambda i, j, k: (k // per, j, k % per))
    elif b_blocks:
        per = shard // tn
        b_spec = pl.BlockSpec((1, tk, tn), lambda i, j, k: (j // per, k, j % per))
    else:
        b_spec = pl.BlockSpec((tn, tk), lambda i, j, k: (j, k)) if tb else pl.BlockSpec((tk, tn), lambda i, j, k: (k, j))
    if out_blocks:
        oper = out_blocks // tn
        o_spec = pl.BlockSpec((1, tm, tn), lambda i, j, k: (j // oper, i, j % oper))
        out_shape = jax.ShapeDtypeStruct((N // out_blocks, M, out_blocks), out_dtype)
    else:
        o_spec = pl.BlockSpec((tm, tn), lambda i, j, k: (i, j))
        out_shape = jax.ShapeDtypeStruct((M, N), out_dtype)
    in_specs = [a_spec, b_spec] + ([o_spec] if has_add else [])
    args = (a, b) + ((add,) if has_add else ())
    return _pcall(
        body, after, name=name, grid=(M // tm, N // tn, nk), in_specs=in_specs, out_specs=o_spec,
        out_shape=out_shape,
        scratch_shapes=[pltpu.VMEM((tm, tn), F32)] if nk > 1 else [],
        compiler_params=_params("parallel", "parallel", "arbitrary"),
    )(*args)


def _rms_fwd(x, g, *, name, tm=512, after=None):
    S, D = x.shape

    def body(x_ref, g_ref, h_ref):
        xv = x_ref[...]
        r = lax.rsqrt(jnp.mean(xv * xv, axis=-1, keepdims=True) + RMS_EPS)
        h_ref[...] = ((xv * r) * g_ref[...]).astype(h_ref.dtype)

    row = pl.BlockSpec((tm, D), lambda i: (i, 0))
    return _pcall(body, after, name=name, grid=(S // tm,), in_specs=[row, pl.BlockSpec((1, D), lambda i: (0, 0))],
                  out_specs=row, out_shape=jax.ShapeDtypeStruct((S, D), _CD), compiler_params=_params("parallel"))(x, g)


def _rms_bwd(x, g, dh, dres, *, name, tm=512, after=None):
    S, D = x.shape

    def body(x_ref, g_ref, dh_ref, dres_ref, dx_ref, dg_ref):
        xv = x_ref[...]
        r = lax.rsqrt(jnp.mean(xv * xv, axis=-1, keepdims=True) + RMS_EPS)
        xh = xv * r
        dhv = dh_ref[...]
        dxh = dhv * g_ref[...]
        dx_ref[...] = dres_ref[...] + r * (dxh - xh * jnp.mean(dxh * xh, axis=-1, keepdims=True))
        part = jnp.sum(dhv * xh, axis=0, keepdims=True)

        @pl.when(pl.program_id(0) == 0)
        def _():
            dg_ref[...] = part

        @pl.when(pl.program_id(0) > 0)
        def _():
            dg_ref[...] += part

    row = pl.BlockSpec((tm, D), lambda i: (i, 0))
    vec = pl.BlockSpec((1, D), lambda i: (0, 0))
    return _pcall(body, after, name=name, grid=(S // tm,), in_specs=[row, vec, row, row], out_specs=[row, vec],
                  out_shape=[jax.ShapeDtypeStruct((S, D), F32), jax.ShapeDtypeStruct((1, D), F32)],
                  compiler_params=_params("arbitrary"))(x, g, dh, dres)


def _loss_head(x, g, tgt, *, name, tm=512):
    S, D = x.shape

    def body(x_ref, g_ref, t_ref, loss_ref, dx_ref, dg_ref):
        xv = x_ref[...]
        gv = g_ref[...]
        r = lax.rsqrt(jnp.mean(xv * xv, axis=-1, keepdims=True) + RMS_EPS)
        xh = xv * r
        err = xh * gv - t_ref[...]
        lpart = 0.5 * jnp.sum(jnp.mean(err * err, axis=-1, keepdims=True), axis=0, keepdims=True)
        dy = err * (1.0 / D)
        dxh = dy * gv
        dx_ref[...] = r * (dxh - xh * jnp.mean(dxh * xh, axis=-1, keepdims=True))
        gpart = jnp.sum(dy * xh, axis=0, keepdims=True)

        @pl.when(pl.program_id(0) == 0)
        def _():
            loss_ref[...] = lpart
            dg_ref[...] = gpart

        @pl.when(pl.program_id(0) > 0)
        def _():
            loss_ref[...] += lpart
            dg_ref[...] += gpart

    row = pl.BlockSpec((tm, D), lambda i: (i, 0))
    vec = pl.BlockSpec((1, D), lambda i: (0, 0))
    one = pl.BlockSpec((1, 1), lambda i: (0, 0))
    return _pcall(body, name=name, grid=(S // tm,), in_specs=[row, vec, row], out_specs=[one, row, vec],
                  out_shape=[jax.ShapeDtypeStruct((1, 1), F32), jax.ShapeDtypeStruct((S, D), F32),
                             jax.ShapeDtypeStruct((1, D), F32)],
                  compiler_params=_params("arbitrary"))(x, g, tgt)


def _sigmoid(z):
    return 1.0 / (1.0 + jnp.exp(-z))


def _gate_fwd(gl, bg, ya, yb, *, name, tm=512):
    S, D = ya.shape

    def body(za_ref, zb_ref, ba_ref, bb_ref, ya_ref, yb_ref, o_ref):
        ga = _sigmoid(za_ref[...].astype(F32) + ba_ref[...])
        gb = _sigmoid(zb_ref[...].astype(F32) + bb_ref[...])
        o_ref[...] = (ga * ya_ref[...].astype(F32) + gb * yb_ref[...].astype(F32)).astype(o_ref.dtype)

    lo = pl.BlockSpec((tm, D), lambda i: (i, 0))
    hi = pl.BlockSpec((tm, D), lambda i: (i, 1))
    vlo = pl.BlockSpec((1, D), lambda i: (0, 0))
    vhi = pl.BlockSpec((1, D), lambda i: (0, 1))
    return _pcall(body, name=name, grid=(S // tm,), in_specs=[lo, hi, vlo, vhi, lo, lo], out_specs=lo,
                  out_shape=jax.ShapeDtypeStruct((S, D), _CD), compiler_params=_params("parallel"))(gl, gl, bg, bg, ya, yb)


def _gate_bwd(dm, gl, bg, ya, yb, *, name, tm=256):
    S, D = ya.shape

    def body(dm_ref, za_ref, zb_ref, ba_ref, bb_ref, ya_ref, yb_ref, dya_ref, dyb_ref, dgl_ref, dbg_ref):
        dmv = dm_ref[...].astype(F32)
        ga = _sigmoid(za_ref[...].astype(F32) + ba_ref[...])
        gb = _sigmoid(zb_ref[...].astype(F32) + bb_ref[...])
        dya_ref[...] = (dmv * ga).astype(dya_ref.dtype)
        dyb_ref[...] = (dmv * gb).astype(dyb_ref.dtype)
        dza = dmv * ya_ref[...].astype(F32) * ga * (1.0 - ga)
        dzb = dmv * yb_ref[...].astype(F32) * gb * (1.0 - gb)
        dgl_ref[:, :D] = dza.astype(dgl_ref.dtype)
        dgl_ref[:, D:] = dzb.astype(dgl_ref.dtype)
        pa = jnp.sum(dza, axis=0, keepdims=True)
        pb = jnp.sum(dzb, axis=0, keepdims=True)

        @pl.when(pl.program_id(0) == 0)
        def _():
            dbg_ref[:, :D] = pa
            dbg_ref[:, D:] = pb

        @pl.when(pl.program_id(0) > 0)
        def _():
            dbg_ref[:, :D] += pa
            dbg_ref[:, D:] += pb

    lo = pl.BlockSpec((tm, D), lambda i: (i, 0))
    hi = pl.BlockSpec((tm, D), lambda i: (i, 1))
    vlo = pl.BlockSpec((1, D), lambda i: (0, 0))
    vhi = pl.BlockSpec((1, D), lambda i: (0, 1))
    wide = pl.BlockSpec((tm, 2 * D), lambda i: (i, 0))
    vwide = pl.BlockSpec((1, 2 * D), lambda i: (0, 0))
    return _pcall(body, name=name, grid=(S // tm,), in_specs=[lo, lo, hi, vlo, vhi, lo, lo],
                  out_specs=[lo, lo, wide, vwide],
                  out_shape=[jax.ShapeDtypeStruct((S, D), _CD), jax.ShapeDtypeStruct((S, D), _CD),
                             jax.ShapeDtypeStruct((S, 2 * D), _CD), jax.ShapeDtypeStruct((1, 2 * D), F32)],
                  compiler_params=_params("arbitrary"))(dm, gl, gl, bg, bg, ya, yb)


def _swiglu_fwd(gu, *, name, tm=256):
    S, F2 = gu.shape
    F = F2 // 2

    def body(g_ref, u_ref, o_ref):
        gv = g_ref[...].astype(F32)
        o_ref[...] = (gv * _sigmoid(gv) * u_ref[...].astype(F32)).astype(o_ref.dtype)

    lo = pl.BlockSpec((tm, F), lambda i: (i, 0))
    hi = pl.BlockSpec((tm, F), lambda i: (i, 1))
    return _pcall(body, name=name, grid=(S // tm,), in_specs=[lo, hi], out_specs=lo,
                  out_shape=jax.ShapeDtypeStruct((S, F), _CD), compiler_params=_params("parallel"))(gu, gu)


def _swiglu_bwd(dact, gu, *, name, tm=256):
    S, F2 = gu.shape
    F = F2 // 2

    def body(d_ref, g_ref, u_ref, o_ref):
        dv = d_ref[...].astype(F32)
        gv = g_ref[...].astype(F32)
        sg = _sigmoid(gv)
        o_ref[:, :F] = (dv * u_ref[...].astype(F32) * (sg * (1.0 + gv * (1.0 - sg)))).astype(o_ref.dtype)
        o_ref[:, F:] = (dv * (gv * sg)).astype(o_ref.dtype)

    lo = pl.BlockSpec((tm, F), lambda i: (i, 0))
    hi = pl.BlockSpec((tm, F), lambda i: (i, 1))
    return _pcall(body, name=name, grid=(S // tm,), in_specs=[lo, lo, hi],
                  out_specs=pl.BlockSpec((tm, F2), lambda i: (i, 0)),
                  out_shape=jax.ShapeDtypeStruct((S, F2), _CD), compiler_params=_params("parallel"))(dact, gu, gu)


def _split3(x):
    hi = x.astype(jnp.bfloat16)
    r1 = x - hi.astype(F32)
    mid = r1.astype(jnp.bfloat16)
    lo = (r1 - mid.astype(F32)).astype(jnp.bfloat16)
    return hi, mid, lo


def _ones_dot_left(ones, x):
    return sum(jnp.dot(ones, p, preferred_element_type=F32) for p in _split3(x))


def _ones_dot_right(x, ones):
    return sum(jnp.dot(p, ones, preferred_element_type=F32) for p in _split3(x))


def _head_sum(x):
    n = x.shape[1]
    r = lax.broadcasted_iota(jnp.int32, (n, n), 0) // HEAD_DIM
    c = lax.broadcasted_iota(jnp.int32, (n, n), 1) // HEAD_DIM
    return _ones_dot_right(x, (r == c).astype(jnp.bfloat16))


def _log_sigmoid(z):
    e = jnp.exp(-jnp.abs(z))
    t = 1.0 + e
    log1p_e = jnp.where(t == 1.0, e, jnp.log(t) * (e / jnp.where(t == 1.0, 1.0, t - 1.0)))
    return jnp.minimum(z, 0.0) - log1p_e


def _fox_cumsum(zf, bf, *, name):
    S, W = zf.shape
    nb = S // 128

    def body(z_ref, b_ref, c_ref):
        tri = (lax.broadcasted_iota(jnp.int32, (128, 128), 0) >= lax.broadcasted_iota(jnp.int32, (128, 128), 1))
        tri = tri.astype(jnp.bfloat16)

        def step(i, carry):
            rows = pl.ds(pl.multiple_of(i * 128, 128), 128)
            lf = _log_sigmoid(z_ref[rows, :] + b_ref[...])
            cb = _ones_dot_left(tri, lf) + carry
            c_ref[rows, :] = cb
            return cb[127:128, :]

        lax.fori_loop(0, nb, step, jnp.zeros((1, W), F32))

    return _pcall(body, name=name, out_shape=jax.ShapeDtypeStruct((S, W), F32),
                  compiler_params=pltpu.CompilerParams(vmem_limit_bytes=VMEM_LIMIT))(zf, bf)


def _fox_cumsum_bwd(dc, zf, bf, *, name):
    S, W = zf.shape
    nb = S // 128

    def body(dc_ref, z_ref, b_ref, dz_ref, db_ref):
        tri = (lax.broadcasted_iota(jnp.int32, (128, 128), 0) <= lax.broadcasted_iota(jnp.int32, (128, 128), 1))
        tri = tri.astype(jnp.bfloat16)

        def step(k, carry):
            tail, acc = carry
            i = nb - 1 - k
            rows = pl.ds(pl.multiple_of(i * 128, 128), 128)
            dlf = _ones_dot_left(tri, dc_ref[rows, :]) + tail
            dz = dlf * _sigmoid(-(z_ref[rows, :] + b_ref[...]))
            dz_ref[rows, :] = dz
            return dlf[0:1, :], acc + jnp.sum(dz, axis=0, keepdims=True)

        _, acc = lax.fori_loop(0, nb, step, (jnp.zeros((1, W), F32), jnp.zeros((1, W), F32)))
        db_ref[...] = acc

    return _pcall(body, name=name,
                  out_shape=[jax.ShapeDtypeStruct((S, W), F32), jax.ShapeDtypeStruct((1, W), F32)],
                  compiler_params=pltpu.CompilerParams(vmem_limit_bytes=VMEM_LIMIT))(dc, zf, bf)


def _dil_slopes(group):
    h = np.arange(1, N_DIL_GROUPS * DIL_HEADS + 1, dtype=np.float32)
    s = (np.float32(2.0) ** (np.float32(-8.0) * h / np.float32(N_DIL_GROUPS * DIL_HEADS))).astype(np.float32)
    return [float(v) for v in s.reshape(N_DIL_GROUPS, DIL_HEADS)[group]]


def _dil_tiles(i, n, blocks_per_seq):
    qi = lax.broadcasted_iota(jnp.int32, (DIL_W, DIL_W), 0)
    kj = lax.broadcasted_iota(jnp.int32, (DIL_W, DIL_W), 1)
    first = ((4 * n + i) % blocks_per_seq) == 0
    valid_prev = jnp.logical_and(kj >= qi, jnp.logical_not(first))
    valid_cur = kj <= qi
    rel_prev = (qi - kj + DIL_W).astype(F32)
    rel_cur = (qi - kj).astype(F32)
    return valid_prev, valid_cur, rel_prev, rel_cur


CHUNK = 4 * DIL_W


def _dil_fwd(q, k, v, group, *, name):
    S = q.shape[0]
    dilation = DIL_PAIRS[group][1]
    bps = (S // dilation) // DIL_W
    slopes = _dil_slopes(group)
    nt = (((1,), (1,)), ((), ()))

    def body(q_ref, k_ref, v_ref, kp_ref, vp_ref, o_ref, l_ref):
        n = pl.program_id(0)
        for i in range(4):
            valid_prev, valid_cur, rel_prev, rel_cur = _dil_tiles(i, n, bps)
            rows = slice(i * DIL_W, (i + 1) * DIL_W)
            prow = slice((i - 1) * DIL_W, i * DIL_W)
            for h in range(DIL_HEADS):
                cols = slice(h * HEAD_DIM, (h + 1) * HEAD_DIM)
                qh = q_ref[rows, cols]
                kc, vc = k_ref[rows, cols], v_ref[rows, cols]
                kp = kp_ref[:, cols] if i == 0 else k_ref[prow, cols]
                vp = vp_ref[:, cols] if i == 0 else v_ref[prow, cols]
                sl = slopes[h] * dilation
                sp = lax.dot_general(qh, kp, nt, preferred_element_type=F32) * ATTN_SCALE - sl * rel_prev
                sc = lax.dot_general(qh, kc, nt, preferred_element_type=F32) * ATTN_SCALE - sl * rel_cur
                sp = jnp.where(valid_prev, sp, NEG_INF)
                sc = jnp.where(valid_cur, sc, NEG_INF)
                m = jnp.maximum(jnp.max(sp, axis=-1, keepdims=True), jnp.max(sc, axis=-1, keepdims=True))
                pp, pc = jnp.exp(sp - m), jnp.exp(sc - m)
                den = jnp.sum(pp, axis=-1, keepdims=True) + jnp.sum(pc, axis=-1, keepdims=True)
                acc = (jnp.dot(pp.astype(_CD), vp, preferred_element_type=F32)
                       + jnp.dot(pc.astype(_CD), vc, preferred_element_type=F32))
                o_ref[rows, cols] = acc / den
                l_ref[rows, cols] = jnp.broadcast_to(m + jnp.log(den), (DIL_W, HEAD_DIM))

    cur = pl.BlockSpec((CHUNK, DIL_OUT), lambda n: (n, 0))
    prev = pl.BlockSpec((DIL_W, DIL_OUT), lambda n: (jnp.maximum(4 * n - 1, 0), 0))
    return _pcall(body, name=name, grid=(S // CHUNK,), in_specs=[cur, cur, cur, prev, prev], out_specs=[cur, cur],
                  out_shape=[jax.ShapeDtypeStruct((S, DIL_OUT), F32), jax.ShapeDtypeStruct((S, DIL_OUT), F32)],
                  compiler_params=_params("parallel"))(q, k, v, k, v)


def _dil_bwd(q, k, v, o, lse, do, dlse, group, *, name):
    S = q.shape[0]
    dilation = DIL_PAIRS[group][1]
    bps = (S // dilation) // DIL_W
    slopes = _dil_slopes(group)
    nchunk = S // CHUNK
    nt = (((1,), (1,)), ((), ()))
    tn = (((0,), (0,)), ((), ()))

    def body(q_ref, k_ref, v_ref, kp_ref, vp_ref, o_ref, l_ref, do_ref, dl_ref, dq_ref, dk_ref, dv_ref,
             dk_s, dv_s):
        step = pl.program_id(0)
        n = nchunk - 1 - step

        @pl.when(step == 0)
        def _():
            dk_s[CHUNK:, :] = jnp.zeros((DIL_W, DIL_OUT), F32)
            dv_s[CHUNK:, :] = jnp.zeros((DIL_W, DIL_OUT), F32)

        dk_s[:CHUNK, :] = jnp.zeros((CHUNK, DIL_OUT), F32)
        dv_s[:CHUNK, :] = jnp.zeros((CHUNK, DIL_OUT), F32)
        for i in range(4):
            valid_prev, valid_cur, rel_prev, rel_cur = _dil_tiles(i, n, bps)
            rows = slice(i * DIL_W, (i + 1) * DIL_W)
            prow = slice((i - 1) * DIL_W, i * DIL_W)
            s_prev = slice(i * DIL_W, (i + 1) * DIL_W)
            s_cur = slice((i + 1) * DIL_W, (i + 2) * DIL_W)
            for h in range(DIL_HEADS):
                cols = slice(h * HEAD_DIM, (h + 1) * HEAD_DIM)
                qh = q_ref[rows, cols]
                kc, vc = k_ref[rows, cols], v_ref[rows, cols]
                kp = kp_ref[:, cols] if i == 0 else k_ref[prow, cols]
                vp = vp_ref[:, cols] if i == 0 else v_ref[prow, cols]
                sl = slopes[h] * dilation
                lh = l_ref[rows, h * HEAD_DIM:h * HEAD_DIM + 1]
                sp = lax.dot_general(qh, kp, nt, preferred_element_type=F32) * ATTN_SCALE - sl * rel_prev
                sc = lax.dot_general(qh, kc, nt, preferred_element_type=F32) * ATTN_SCALE - sl * rel_cur
                pp = jnp.exp(jnp.where(valid_prev, sp, NEG_INF) - lh)
                pc = jnp.exp(jnp.where(valid_cur, sc, NEG_INF) - lh)
                doh = do_ref[rows, cols]
                dsum = jnp.sum(doh * o_ref[rows, cols], axis=-1, keepdims=True)
                shift = dl_ref[rows, h * HEAD_DIM:h * HEAD_DIM + 1] - dsum
                dob = doh.astype(_CD)
                dsp = pp * (lax.dot_general(dob, vp, nt, preferred_element_type=F32) + shift)
                dsc = pc * (lax.dot_general(dob, vc, nt, preferred_element_type=F32) + shift)
                dspb = (dsp * ATTN_SCALE).astype(_CD)
                dscb = (dsc * ATTN_SCALE).astype(_CD)
                dq_ref[rows, cols] = (jnp.dot(dspb, kp, preferred_element_type=F32)
                                      + jnp.dot(dscb, kc, preferred_element_type=F32)).astype(dq_ref.dtype)
                dk_s[s_prev, cols] += lax.dot_general(dspb, qh, tn, preferred_element_type=F32)
                dk_s[s_cur, cols] += lax.dot_general(dscb, qh, tn, preferred_element_type=F32)
                dv_s[s_prev, cols] += lax.dot_general(pp.astype(_CD), dob, tn, preferred_element_type=F32)
                dv_s[s_cur, cols] += lax.dot_general(pc.astype(_CD), dob, tn, preferred_element_type=F32)
        dk_ref[...] = dk_s[DIL_W:, :].astype(dk_ref.dtype)
        dv_ref[...] = dv_s[DIL_W:, :].astype(dv_ref.dtype)
        dk_s[CHUNK:, :] = dk_s[:DIL_W, :]
        dv_s[CHUNK:, :] = dv_s[:DIL_W, :]

    cur = pl.BlockSpec((CHUNK, DIL_OUT), lambda s: (nchunk - 1 - s, 0))
    prev = pl.BlockSpec((DIL_W, DIL_OUT), lambda s: (jnp.maximum(4 * (nchunk - 1 - s) - 1, 0), 0))
    shp = jax.ShapeDtypeStruct((S, DIL_OUT), _CD)
    return _pcall(body, name=name, grid=(nchunk,), in_specs=[cur, cur, cur, prev, prev, cur, cur, cur, cur],
                  out_specs=[cur, cur, cur], out_shape=[shp, shp, shp],
                  scratch_shapes=[pltpu.VMEM((CHUNK + DIL_W, DIL_OUT), F32), pltpu.VMEM((CHUNK + DIL_W, DIL_OUT), F32)],
                  compiler_params=_params("arbitrary"))(q, k, v, k, v, o, lse, do, dlse)


def _dil_mix_fwd(os_, ls_, *, name, tm=512):
    S, W = os_[0].shape

    def body(o0, o1, o2, l0, l1, l2, out_ref):
        ls = [l0[...], l1[...], l2[...]]
        m = jnp.maximum(jnp.maximum(ls[0], ls[1]), ls[2])
        es = [jnp.exp(l - m) for l in ls]
        den = es[0] + es[1] + es[2]
        out_ref[...] = ((es[0] * o0[...] + es[1] * o1[...] + es[2] * o2[...]) / den).astype(out_ref.dtype)

    row = pl.BlockSpec((tm, W), lambda i: (i, 0))
    return _pcall(body, name=name, grid=(S // tm,), in_specs=[row] * 6, out_specs=row,
                  out_shape=jax.ShapeDtypeStruct((S, W), _CD), compiler_params=_params("parallel"))(*os_, *ls_)


def _dil_mix_bwd(doa, os_, ls_, *, name, tm=512, after=None):
    S, W = doa.shape

    def body(d_ref, o0, o1, o2, l0, l1, l2, do0, do1, do2, dl0, dl1, dl2):
        dv = d_ref[...]
        ls = [l0[...], l1[...], l2[...]]
        m = jnp.maximum(jnp.maximum(ls[0], ls[1]), ls[2])
        es = [jnp.exp(l - m) for l in ls]
        den = es[0] + es[1] + es[2]
        al = [e / den for e in es]
        da = [_head_sum(dv * o[...]) for o in (o0, o1, o2)]
        mean = al[0] * da[0] + al[1] * da[1] + al[2] * da[2]
        for a, d_, do_ref, dl_ref in zip(al, da, (do0, do1, do2), (dl0, dl1, dl2)):
            do_ref[...] = a * dv
            dl_ref[...] = a * (d_ - mean)

    row = pl.BlockSpec((tm, W), lambda i: (i, 0))
    shp = jax.ShapeDtypeStruct((S, W), F32)
    return _pcall(body, after, name=name, grid=(S // tm,), in_specs=[row] * 7, out_specs=[row] * 6, out_shape=[shp] * 6,
                  compiler_params=_params("parallel"))(doa, *os_, *ls_)


FOX_T = 512


PACK = 2 * HEAD_DIM
HEAD_PAIRS = N_FOX_HEADS // 2
Q_BLOCK0 = (3 * DIL_WIDTH) // PACK
K_BLOCK0 = (3 * DIL_WIDTH + FOX_WIDTH) // PACK
V_BLOCK0 = (3 * DIL_WIDTH + 2 * FOX_WIDTH) // PACK


def _pieces(x):
    hi = x.astype(jnp.bfloat16).astype(F32)
    r = x - hi
    mid = r.astype(jnp.bfloat16).astype(F32)
    lo = (r - mid).astype(jnp.bfloat16).astype(F32)
    return [hi, mid, lo]


def _extras(first, second, rows):
    lane = lax.broadcasted_iota(jnp.int32, (rows, HEAD_DIM), 1)
    out = jnp.zeros((rows, HEAD_DIM), F32)
    for idx, val in enumerate(list(first) + list(second)):
        out = jnp.where(lane == idx, val, out)
    return out


def _head_column(c, h):
    lane = lax.broadcasted_iota(jnp.int32, c.shape, 1)
    return jnp.sum(jnp.where(lane == h, c, 0.0), axis=1, keepdims=True)


ONES3 = [1.0, 1.0, 1.0]
ZEROS3 = [0.0, 0.0, 0.0]


def _fox_pack_fwd(qkv, c, *, name, tm=512):
    S = qkv.shape[0]

    def body(q_ref, k_ref, v_ref, c_ref, qo_ref, ko_ref, vo_ref):
        hp = pl.program_id(1)
        cv = c_ref[...]
        for hh in range(2):
            ch = _pieces(_head_column(cv, 2 * hp + hh))
            src = slice(hh * HEAD_DIM, (hh + 1) * HEAD_DIM)
            lo = slice(hh * PACK, hh * PACK + HEAD_DIM)
            hi = slice(hh * PACK + HEAD_DIM, (hh + 1) * PACK)
            qo_ref[:, lo] = (q_ref[:, src].astype(F32) * ATTN_SCALE).astype(qo_ref.dtype)
            qo_ref[:, hi] = _extras(ch, ONES3, tm).astype(qo_ref.dtype)
            ko_ref[:, lo] = k_ref[:, src]
            ko_ref[:, hi] = _extras(ONES3, [-p for p in ch], tm).astype(ko_ref.dtype)
            vo_ref[:, lo] = v_ref[:, src]
            vo_ref[:, hi] = _extras(ONES3, ZEROS3, tm).astype(vo_ref.dtype)

    def src(block0):
        return pl.BlockSpec((tm, PACK), lambda i, hp: (i, block0 + hp))

    out = pl.BlockSpec((tm, 2 * PACK), lambda i, hp: (i, hp))
    shp = jax.ShapeDtypeStruct((S, N_FOX_HEADS * PACK), _CD)
    return _pcall(body, name=name, grid=(S // tm, HEAD_PAIRS),
                  in_specs=[src(Q_BLOCK0), src(K_BLOCK0), src(V_BLOCK0), pl.BlockSpec((tm, PACK), lambda i, hp: (i, 0))],
                  out_specs=[out, out, out], out_shape=[shp, shp, shp],
                  compiler_params=_params("parallel", "parallel"))(qkv, qkv, qkv, c)


def _fox_fwd(qp, kp, vp, *, name):
    S = qp.shape[0]
    nt = S // FOX_T
    nt_dims = (((1,), (1,)), ((), ()))
    tn_dims = (((0,), (0,)), ((), ()))

    def body(i_tab, j_tab, q_ref, k_ref, v_ref, o_ref, l_ref, m_s, acc_s):
        t = pl.program_id(1)
        i, j = i_tab[t], j_tab[t]

        @pl.when(j == 0)
        def _():
            m_s[...] = jnp.full((2, 1, FOX_T), NEG_INF, F32)
            acc_s[...] = jnp.zeros((2, PACK, FOX_T), F32)

        def tile(diagonal):
            for hh in range(2):
                cols = slice(hh * PACK, (hh + 1) * PACK)
                st = lax.dot_general(k_ref[:, cols], q_ref[:, cols], nt_dims, preferred_element_type=F32)
                if diagonal:
                    key = lax.broadcasted_iota(jnp.int32, (FOX_T, FOX_T), 0)
                    qry = lax.broadcasted_iota(jnp.int32, (FOX_T, FOX_T), 1)
                    st = jnp.where(key <= qry, st, NEG_INF)
                m_old = m_s[hh]
                m_new = jnp.maximum(m_old, jnp.max(st, axis=0, keepdims=True))
                pt = jnp.exp(st - m_new)
                acc_s[hh] = jnp.exp(m_old - m_new) * acc_s[hh] + lax.dot_general(
                    v_ref[:, cols], pt.astype(_CD), tn_dims, preferred_element_type=F32)
                m_s[hh] = m_new

        @pl.when(j < i)
        def _():
            tile(False)

        @pl.when(j == i)
        def _():
            tile(True)
            for hh in range(2):
                acc = acc_s[hh]
                den = acc[HEAD_DIM:HEAD_DIM + 1, :]
                cols = slice(hh * HEAD_DIM, (hh + 1) * HEAD_DIM)
                o_ref[:, cols] = (acc[:HEAD_DIM, :] / den).T
                l_ref[:, cols] = jnp.broadcast_to(m_s[hh] + jnp.log(den), (HEAD_DIM, FOX_T)).T

    pairs = [(i, j) for i in range(nt) for j in range(i + 1)]
    i_tab = jnp.asarray([p[0] for p in pairs], jnp.int32)
    j_tab = jnp.asarray([p[1] for p in pairs], jnp.int32)
    qs = pl.BlockSpec((FOX_T, 2 * PACK), lambda hp, t, it, jt: (it[t], hp))
    ks = pl.BlockSpec((FOX_T, 2 * PACK), lambda hp, t, it, jt: (jt[t], hp))
    os_ = pl.BlockSpec((FOX_T, PACK), lambda hp, t, it, jt: (it[t], hp))
    shp = jax.ShapeDtypeStruct((S, FOX_WIDTH), F32)
    grid_spec = pltpu.PrefetchScalarGridSpec(
        num_scalar_prefetch=2, grid=(HEAD_PAIRS, len(pairs)), in_specs=[qs, ks, ks], out_specs=[os_, os_],
        scratch_shapes=[pltpu.VMEM((2, 1, FOX_T), F32), pltpu.VMEM((2, PACK, FOX_T), F32)])
    return _pcall(body, name=name, grid_spec=grid_spec, out_shape=[shp, shp],
                  compiler_params=_params("parallel", "arbitrary"))(i_tab, j_tab, qp, kp, vp)


def _fox_pack_bwd(qkv, c, o, lse, do, *, name, tm=512, after=None):
    S = qkv.shape[0]

    def body(q_ref, c_ref, o_ref, l_ref, do_ref, qo_ref, do_out_ref):
        hp = pl.program_id(1)
        cv = c_ref[...]
        for hh in range(2):
            src = slice(hh * HEAD_DIM, (hh + 1) * HEAD_DIM)
            lo = slice(hh * PACK, hh * PACK + HEAD_DIM)
            hi = slice(hh * PACK + HEAD_DIM, (hh + 1) * PACK)
            shift = _head_column(cv, 2 * hp + hh) - l_ref[:, hh * HEAD_DIM:hh * HEAD_DIM + 1]
            dov = do_ref[:, src]
            dsum = jnp.sum(dov * o_ref[:, src], axis=-1, keepdims=True)
            qo_ref[:, lo] = (q_ref[:, src].astype(F32) * ATTN_SCALE).astype(qo_ref.dtype)
            qo_ref[:, hi] = _extras(_pieces(shift), ONES3, tm).astype(qo_ref.dtype)
            do_out_ref[:, lo] = dov.astype(do_out_ref.dtype)
            do_out_ref[:, hi] = _extras(_pieces(-dsum), ZEROS3, tm).astype(do_out_ref.dtype)

    pair = pl.BlockSpec((tm, PACK), lambda i, hp: (i, hp))
    out = pl.BlockSpec((tm, 2 * PACK), lambda i, hp: (i, hp))
    shp = jax.ShapeDtypeStruct((S, N_FOX_HEADS * PACK), _CD)
    return _pcall(body, after, name=name, grid=(S // tm, HEAD_PAIRS),
                  in_specs=[pl.BlockSpec((tm, PACK), lambda i, hp: (i, Q_BLOCK0 + hp)),
                            pl.BlockSpec((tm, PACK), lambda i, hp: (i, 0)), pair, pair, pair],
                  out_specs=[out, out], out_shape=[shp, shp],
                  compiler_params=_params("parallel", "parallel"))(qkv, c, o, lse, do)


def _fox_bwd(qp, kp, vp, dop, *, name):
    S = qp.shape[0]
    nt = S // FOX_T
    nt_dims = (((1,), (1,)), ((), ()))
    tn_dims = (((0,), (0,)), ((), ()))

    def body(i_tab, j_tab, q_ref, k_ref, v_ref, do_ref, dq_ref, dk_ref, dv_ref, dc_ref, dr_ref,
             dq_s, dk_s, dv_s, dc_s, dr_s):
        t = pl.program_id(1)
        i, j = i_tab[t], j_tab[t]

        @pl.when(t == 0)
        def _():
            dq_s[...] = jnp.zeros((S, 2 * PACK), F32)
            dr_s[...] = jnp.zeros((2, 1, S), F32)

        @pl.when(i == j)
        def _():
            dk_s[...] = jnp.zeros((FOX_T, 2 * PACK), F32)
            dv_s[...] = jnp.zeros((FOX_T, 2 * PACK), F32)
            dc_s[...] = jnp.zeros((2, FOX_T, 1), F32)

        def tile(diagonal):
            rows = pl.ds(pl.multiple_of(i * FOX_T, FOX_T), FOX_T)
            for hh in range(2):
                cols = slice(hh * PACK, (hh + 1) * PACK)
                qv, kv, vv, dov = q_ref[:, cols], k_ref[:, cols], v_ref[:, cols], do_ref[:, cols]
                pt = jnp.exp(lax.dot_general(kv, qv, nt_dims, preferred_element_type=F32))
                if diagonal:
                    key = lax.broadcasted_iota(jnp.int32, (FOX_T, FOX_T), 0)
                    qry = lax.broadcasted_iota(jnp.int32, (FOX_T, FOX_T), 1)
                    pt = jnp.where(key <= qry, pt, 0.0)
                dst = pt * lax.dot_general(vv, dov, nt_dims, preferred_element_type=F32)
                dsb = dst.astype(_CD)
                dc_s[hh] += jnp.sum(dst, axis=1, keepdims=True)
                dr_s[hh, :, rows] += jnp.sum(dst, axis=0, keepdims=True)
                dv_s[:, cols] += jnp.dot(pt.astype(_CD), dov, preferred_element_type=F32)
                dk_s[:, cols] += jnp.dot(dsb, qv, preferred_element_type=F32)
                dq_s[rows, cols] += lax.dot_general(dsb, kv, tn_dims, preferred_element_type=F32)

        @pl.when(i > j)
        def _():
            tile(False)

        @pl.when(i == j)
        def _():
            tile(True)

        @pl.when(i == nt - 1)
        def _():
            for hh in range(2):
                src = slice(hh * PACK, hh * PACK + HEAD_DIM)
                dst_cols = slice(hh * HEAD_DIM, (hh + 1) * HEAD_DIM)
                dk_ref[:, dst_cols] = dk_s[:, src].astype(dk_ref.dtype)
                dv_ref[:, dst_cols] = dv_s[:, src].astype(dv_ref.dtype)
                dc_ref[:, dst_cols] = jnp.broadcast_to(dc_s[hh], (FOX_T, HEAD_DIM))

        @pl.when(t == len(pairs) - 1)
        def _():
            for hh in range(2):
                dq_ref[:, hh * HEAD_DIM:(hh + 1) * HEAD_DIM] = (
                    dq_s[:, hh * PACK:hh * PACK + HEAD_DIM] * ATTN_SCALE).astype(dq_ref.dtype)
            dr_ref[...] = dr_s[...]

    pairs = [(i, j) for j in range(nt) for i in range(j, nt)]
    i_tab = jnp.asarray([p[0] for p in pairs], jnp.int32)
    j_tab = jnp.asarray([p[1] for p in pairs], jnp.int32)
    qs = pl.BlockSpec((FOX_T, 2 * PACK), lambda hp, t, it, jt: (it[t], hp))
    ks = pl.BlockSpec((FOX_T, 2 * PACK), lambda hp, t, it, jt: (jt[t], hp))
    whole = pl.BlockSpec((S, PACK), lambda hp, t, it, jt: (0, hp))
    cs = pl.BlockSpec((FOX_T, PACK), lambda hp, t, it, jt: (jt[t], hp))
    rs = pl.BlockSpec((2, 1, S), lambda hp, t, it, jt: (hp, 0, 0))
    shp = jax.ShapeDtypeStruct((S, FOX_WIDTH), _CD)
    grid_spec = pltpu.PrefetchScalarGridSpec(
        num_scalar_prefetch=2, grid=(HEAD_PAIRS, len(pairs)), in_specs=[qs, ks, ks, qs],
        out_specs=[whole, cs, cs, cs, rs],
        scratch_shapes=[pltpu.VMEM((S, 2 * PACK), F32), pltpu.VMEM((FOX_T, 2 * PACK), F32),
                        pltpu.VMEM((FOX_T, 2 * PACK), F32), pltpu.VMEM((2, FOX_T, 1), F32),
                        pltpu.VMEM((2, 1, S), F32)])
    return _pcall(body, name=name, grid_spec=grid_spec,
                  out_shape=[shp, shp, shp, jax.ShapeDtypeStruct((S, FOX_WIDTH), F32),
                             jax.ShapeDtypeStruct((N_FOX_HEADS, 1, S), F32)],
                  compiler_params=_params("parallel", "arbitrary"))(i_tab, j_tab, qp, kp, vp, dop)


def _dedilate(t, d):
    if d == 1:
        return t
    S, C = t.shape
    return t.reshape(S // d, d, C).transpose(1, 0, 2).reshape(S, C)


def _redilate(t, d):
    if d == 1:
        return t
    S, C = t.shape
    return t.reshape(d, S // d, C).transpose(1, 0, 2).reshape(S, C)


def _layer_step(x, tgt, w, p, late_weights=None, grad_sink=None, after=None, first_weights=None):
    S = x.shape[0]
    after_norm, after_proj = after if after is not None else (None, None)
    h = _rms_fwd(x, p["norm_mix_g"], name="rms_mix", after=after_norm)
    if first_weights is not None:
        w = {**w, **first_weights(h)}
    qkv = _mm(h, w["qkv"], name="proj_qkv", out_dtype=_CD, tn=768, tm=2048, after=after_proj)
    zf = _mm(h, w["f"], name="proj_f")
    gl = _mm(h, w["g"], name="proj_gate", tn=1024)

    dil_q, dil_k, dil_v = [], [], []
    dil_o, dil_l = [], []
    for g, (_, d) in enumerate(DIL_PAIRS):
        qg = _dedilate(qkv[:, g * DIL_OUT:(g + 1) * DIL_OUT], d)
        kg = _dedilate(qkv[:, DIL_WIDTH + g * DIL_OUT:DIL_WIDTH + (g + 1) * DIL_OUT], d)
        vg = _dedilate(qkv[:, 2 * DIL_WIDTH + g * DIL_OUT:2 * DIL_WIDTH + (g + 1) * DIL_OUT], d)
        og, lg = _dil_fwd(qg, kg, vg, g, name=f"dil_fwd{g}")
        dil_q.append(qg), dil_k.append(kg), dil_v.append(vg)
        dil_o.append(_redilate(og, d)), dil_l.append(_redilate(lg, d))
    o_a = _dil_mix_fwd(dil_o, dil_l, name="dil_mix")

    c = _fox_cumsum(zf, p["b_fgt"], name="fox_cumsum")
    fqp, fkp, fvp = _fox_pack_fwd(qkv, c, name="fox_pack")
    o_b, flse = _fox_fwd(fqp, fkp, fvp, name="fox_fwd")

    if late_weights is not None:
        w = {**w, **late_weights(o_b)}
    y_a = _mm(o_a, w["dil_out"], name="y_a", tn=1024, out_dtype=_CD)
    y_b = _mm(o_b, w["fox_out"], name="y_b", tn=1024, out_dtype=_CD)
    merged = _gate_fwd(gl, p["b_gate"], y_a, y_b, name="gate_fwd")
    x1 = _mm(merged, w["out"], name="mix_out", add=x)

    h2 = _rms_fwd(x1, p["norm_ffn_g"], name="rms_ffn")
    gu = _mm(h2, w["ffn_in"], name="ffn_in", tn=1408, b_blocks=True, out_dtype=_CD, tm=2048)
    act = _swiglu_fwd(gu, name="swiglu")
    x2 = _mm(act, w["ffn_down"], name="ffn_down", add=x1, tk=2816)

    loss, dx2, dg_final = _loss_head(x2, p["norm_final_g"], tgt, name="loss_head")

    dact = _mm(dx2, w["ffn_down"], name="d_act", tb=True, tn=1408, out_dtype=_CD)
    gw_ffn_down = _mm(act, dx2, name="gw_ffn_down", ta=True, out_dtype=_CD, tm=1408)
    dgu = _swiglu_bwd(dact, gu, name="swiglu_bwd")
    dh2 = _mm(dgu, w["ffn_in"], name="d_h2", tb=True, tk=1408, b_blocks=True, tm=2048)
    gw_ffn_in = _mm(h2, dgu, name="gw_ffn_in", ta=True, out_dtype=_CD, tn=1408, out_blocks=1408)
    sink = grad_sink if grad_sink is not None else (lambda group, grads: None)
    tok = sink("ffn", dict(ffn_in=gw_ffn_in, ffn_down=gw_ffn_down))
    dx1, dg_ffn = _rms_bwd(x1, p["norm_ffn_g"], dh2, dx2, name="rms_ffn_bwd", after=tok)

    dmerged = _mm(dx1, w["out"], name="d_merged", tb=True, out_dtype=_CD)
    gw_out = _mm(merged, dx1, name="gw_out", ta=True, out_dtype=_CD)
    dy_a, dy_b, dgl, db_gate = _gate_bwd(dmerged, gl, p["b_gate"], y_a, y_b, name="gate_bwd")
    do_a = _mm(dy_a, w["dil_out"], name="d_o_a", tb=True)
    gw_dil_out = _mm(o_a, dy_a, name="gw_dil_out", ta=True, out_dtype=_CD, tn=1024)
    do_b = _mm(dy_b, w["fox_out"], name="d_o_b", tb=True)
    gw_fox_out = _mm(o_b, dy_b, name="gw_fox_out", ta=True, out_dtype=_CD, tn=1024)
    tok = sink("mix", dict(dil_out=gw_dil_out, fox_out=gw_fox_out, out=gw_out))

    bqp, bdop = _fox_pack_bwd(qkv, c, o_b, flse, do_b, name="fox_pack_bwd", after=tok)
    dqp, dkp, dvp, dck, dcq = _fox_bwd(bqp, fkp, fvp, bdop, name="fox_bwd")
    dc = dcq[:, 0, :].T - dck.reshape(S, N_FOX_HEADS, HEAD_DIM)[:, :, 0]
    dc = jnp.pad(dc, ((0, 0), (0, F_PAD - N_FOX_HEADS)))
    dzf, db_fgt = _fox_cumsum_bwd(dc, zf, p["b_fgt"], name="fox_cumsum_bwd")

    douts = _dil_mix_bwd(do_a, dil_o, dil_l, name="dil_mix_bwd", after=tok)
    dqs, dks, dvs = [], [], []
    for g, (_, d) in enumerate(DIL_PAIRS):
        dq, dk, dv = _dil_bwd(dil_q[g], dil_k[g], dil_v[g], _dedilate(dil_o[g], d), _dedilate(dil_l[g], d),
                              _dedilate(douts[g], d), _dedilate(douts[3 + g], d), g, name=f"dil_bwd{g}")
        dqs.append(_redilate(dq, d)), dks.append(_redilate(dk, d)), dvs.append(_redilate(dv, d))
    dqkv = jnp.concatenate(dqs + dks + dvs + [dqp, dkp, dvp], axis=1)

    gw_qkv = _mm(h, dqkv, name="gw_qkv", ta=True, out_dtype=_CD, tn=768)
    gw_g = _mm(h, dgl, name="gw_gate", ta=True, out_dtype=_CD)
    gw_f = _mm(h, dzf, name="gw_f", ta=True, out_dtype=_CD)
    tok = sink("in", dict(qkv=gw_qkv, f=gw_f, g=gw_g))
    dh = _mm(dqkv, w["qkv"], name="d_h_qkv", tb=True, tk=1920, tm=2048, after=tok)
    dh = _mm(dgl, w["g"], name="d_h_gate", tb=True, add=dh)
    dh = _mm(dzf, w["f"], name="d_h_f", tb=True, add=dh)
    dx, dg_mix = _rms_bwd(x, p["norm_mix_g"], dh, dx1, name="rms_mix_bwd")

    gw = dict(qkv=gw_qkv, f=gw_f, g=gw_g, dil_out=gw_dil_out, fox_out=gw_fox_out, out=gw_out, ffn_in=gw_ffn_in,
              ffn_down=gw_ffn_down)
    small = dict(norm_mix_g=dg_mix, b_fgt=db_fgt, b_gate=db_gate, norm_ffn_g=dg_ffn, norm_final_g=dg_final)
    return loss, dx, gw, small


def _position():
    return lax.axis_index("x"), lax.axis_index("y"), lax.axis_index("c")


def _other_chips(x, y):
    return [(1 - x, y), (x, 1 - y), (1 - x, 1 - y)]


ROW_TILE = 16


def _row_chunks(rows, want=4):
    n = want
    while n > 1 and rows % (n * ROW_TILE):
        n //= 2
    return n


SEM_SPEC = pl.BlockSpec(memory_space=pltpu.SEMAPHORE)
ANY_SPEC = pl.BlockSpec(memory_space=pl.ANY)
DATAFLOW = pltpu.SideEffectType.DATAFLOW_SIDE_EFFECTING


def _in_hbm(a):
    return pltpu.with_memory_space_constraint(a, pltpu.HBM)


def _split_copy_start(srcs, land_shapes, copies, after, *, name):
    n, m = len(srcs), len(land_shapes)

    def body(*refs):
        src_refs, land_refs = refs[:n], refs[n:n + m]
        send_sems, recv_sems = refs[n + m + 1], refs[n + m + 2]
        token = refs[-1]
        x, y, c = _position()
        for k, (src, dst, peer) in enumerate(copies(x, y, c, src_refs, land_refs)):
            pltpu.make_async_remote_copy(src_ref=src, dst_ref=dst, send_sem=send_sems.at[k], recv_sem=recv_sems.at[k],
                                         device_id=peer, device_id_type=MESH).start()
        token[...] = jnp.zeros_like(token)

    lands = [lax.empty(s.shape, s.dtype) for s in land_shapes]
    count = len(copies(0, 0, 0, srcs, lands))
    out = _pcall(
        body, name=name,
        out_shape=(pltpu.SemaphoreType.DMA((count,)), pltpu.SemaphoreType.DMA((count,)),
                   *[pltpu.HBM(s.shape, s.dtype) for s in srcs], *[pltpu.HBM(s.shape, s.dtype) for s in land_shapes],
                   jax.ShapeDtypeStruct((8, 128), F32)),
        in_specs=[HBM_SPEC] * (n + m) + [ANY_SPEC],
        out_specs=(SEM_SPEC, SEM_SPEC, *[HBM_SPEC] * (n + m), pl.BlockSpec(memory_space=pltpu.VMEM)),
        input_output_aliases={k: 2 + k for k in range(n + m)},
        compiler_params=pltpu.CompilerParams(has_side_effects=DATAFLOW),
    )(*[_in_hbm(s) for s in srcs], *[_in_hbm(l) for l in lands], after)
    return out[0], out[1], list(out[2:2 + n]), list(out[2 + n:2 + n + m]), out[-1]


def _split_copy_wait(send_sems, recv_sems, srcs, lands, copies, after, *, name):
    n, m = len(srcs), len(lands)

    def body(*refs):
        src_refs, land_refs = refs[:n], refs[n:n + m]
        send, recv = refs[n + m], refs[n + m + 1]
        x, y, c = _position()
        for k, (src, dst, peer) in enumerate(copies(x, y, c, src_refs, land_refs)):
            cp = pltpu.make_async_remote_copy(src_ref=src, dst_ref=dst, send_sem=send.at[k], recv_sem=recv.at[k],
                                              device_id=peer, device_id_type=MESH)
            cp.wait_send()
            cp.wait_recv()

    afters = list(after) if isinstance(after, (list, tuple)) else [after]
    out = _pcall(
        body, name=name,
        out_shape=tuple(pltpu.HBM(s.shape, s.dtype) for s in list(srcs) + list(lands)),
        in_specs=[HBM_SPEC] * (n + m) + [SEM_SPEC, SEM_SPEC] + [ANY_SPEC] * len(afters),
        out_specs=tuple([HBM_SPEC] * (n + m)),
        input_output_aliases={k: k for k in range(n + m)},
        compiler_params=pltpu.CompilerParams(has_side_effects=DATAFLOW),
    )(*srcs, *lands, send_sems, recv_sems, *afters)
    return list(out[:n]), list(out[n:])


def _gather_copies(x, y, c, shard_refs, land_refs):
    out = []
    for s, l in zip(shard_refs, land_refs):
        half = s.shape[0] // 2
        nq = _row_chunks(half)
        for cx, cy in _other_chips(x, y):
            for q in range(nq):
                rows = pl.ds(c * half + q * (half // nq), half // nq)
                out.append((s.at[rows, :], l.at[2 * x + y, rows, :], (cx, cy, c)))
    return out


def _scatter_copies(x, y, c, part_refs, land_refs):
    out = []
    for p, l in zip(part_refs, land_refs):
        nq = _row_chunks(p.shape[1])
        for r, (cx, cy) in enumerate(_other_chips(x, y)):
            for q in range(nq):
                rows = pl.ds(q * (p.shape[1] // nq), p.shape[1] // nq)
                out.append((p.at[2 * cx + cy, rows, :], l.at[r, rows, :], (cx, cy, c)))
    return out


def _forward_halves(lands, *, name):
    n = len(lands)

    def body(*refs):
        ins = refs[:n]
        send_sems, recv_sems = refs[2 * n:]
        x, y, c = _position()
        copies = []
        for w in range(n):
            half = ins[w].shape[1] // 2
            for r, (cx, cy) in enumerate(_other_chips(x, y)):
                blk = ins[w].at[2 * cx + cy, pl.ds(c * half, half), :]
                cp = pltpu.make_async_remote_copy(src_ref=blk, dst_ref=blk, send_sem=send_sems.at[w, r],
                                                  recv_sem=recv_sems.at[w, r], device_id=(x, y, 1 - c),
                                                  device_id_type=MESH)
                cp.start()
                copies.append(cp)
        for w in range(n):
            half = ins[w].shape[1] // 2
            for r, (cx, cy) in enumerate(_other_chips(x, y)):
                blk = ins[w].at[2 * cx + cy, pl.ds((1 - c) * half, half), :]
                pltpu.make_async_remote_copy(src_ref=blk, dst_ref=blk, send_sem=send_sems.at[w, r],
                                             recv_sem=recv_sems.at[w, r], device_id=(x, y, 1 - c),
                                             device_id_type=MESH).wait_recv()
        for cp in copies:
            cp.wait_send()

    return _pcall(
        body, name=name, in_specs=[HBM_SPEC] * n, out_specs=[HBM_SPEC] * n,
        out_shape=[jax.ShapeDtypeStruct(l.shape, l.dtype) for l in lands],
        input_output_aliases={k: k for k in range(n)},
        scratch_shapes=[pltpu.SemaphoreType.DMA((n, 3)), pltpu.SemaphoreType.DMA((n, 3))],
    )(*lands)


def _swap_halves(grads, name="swap_halves"):
    n = len(grads)

    def body(*refs):
        ins, outs = refs[:n], refs[n:2 * n]
        send_sems, recv_sems = refs[2 * n:]
        x, y, c = _position()
        copies = []
        for w in range(n):
            half = ins[w].shape[1] // 2
            cp = pltpu.make_async_remote_copy(
                src_ref=ins[w].at[:, pl.ds((1 - c) * half, half), :], dst_ref=outs[w], send_sem=send_sems.at[w],
                recv_sem=recv_sems.at[w], device_id=(x, y, 1 - c), device_id_type=MESH)
            cp.start()
            copies.append(cp)
        for cp in copies:
            cp.wait()

    return _pcall(
        body, name=name, in_specs=[HBM_SPEC] * n, out_specs=[HBM_SPEC] * n,
        out_shape=[jax.ShapeDtypeStruct((4, g.shape[1] // 2, g.shape[2]), g.dtype) for g in grads],
        scratch_shapes=[pltpu.SemaphoreType.DMA((n,)), pltpu.SemaphoreType.DMA((n,))],
    )(*grads)


def _share_halves(halves):
    n = len(halves)

    def body(*refs):
        ins, outs = refs[:n], refs[n:2 * n]
        send_sems, recv_sems = refs[2 * n:]
        x, y, c = _position()
        copies = []
        for w in range(n):
            cp = pltpu.make_async_remote_copy(src_ref=ins[w], dst_ref=outs[w], send_sem=send_sems.at[w],
                                              recv_sem=recv_sems.at[w], device_id=(x, y, 1 - c), device_id_type=MESH)
            cp.start()
            copies.append(cp)
        for cp in copies:
            cp.wait()

    return _pcall(
        body, name="share_halves", in_specs=[HBM_SPEC] * n, out_specs=[HBM_SPEC] * n,
        out_shape=[jax.ShapeDtypeStruct(h.shape, h.dtype) for h in halves],
        scratch_shapes=[pltpu.SemaphoreType.DMA((n,)), pltpu.SemaphoreType.DMA((n,))],
    )(*halves)


def _sum_small(part):
    rows, width = part.shape

    def body(x_ref, out_ref, all_ref, send_sems, recv_sems):
        x, y, c = _position()
        me, sibling = (x, y, c), (x, y, 1 - c)
        chips = _other_chips(x, y)

        def block(px, py, pc):
            return all_ref.at[pl.ds((4 * px + 2 * py + pc) * rows, rows), :]

        def copy(k, blk, to, src=None):
            return pltpu.make_async_remote_copy(
                src_ref=block(*blk) if src is None else src, dst_ref=block(*blk), send_sem=send_sems.at[k],
                recv_sem=recv_sems.at[k], device_id=to, device_id_type=MESH)

        all_ref[pl.ds((4 * x + 2 * y + c) * rows, rows), :] = x_ref[...]
        first = [copy(0, me, sibling, src=x_ref)]
        first += [copy(1 + j, me, (*chip, c), src=x_ref) for j, chip in enumerate(chips)]
        for cp in first:
            cp.start()
        passed = [copy(4 + j, (*chip, c), sibling) for j, chip in enumerate(chips)]
        for j, chip in enumerate(chips):
            copy(1 + j, (*chip, c), me).wait_recv()
            passed[j].start()
        copy(0, sibling, me).wait_recv()
        for j, chip in enumerate(chips):
            copy(4 + j, (*chip, 1 - c), me).wait_recv()
        for cp in first + passed:
            cp.wait_send()
        total = all_ref[0:rows, :]
        for d in range(1, 8):
            total = total + all_ref[d * rows:(d + 1) * rows, :]
        out_ref[...] = total

    vm = pl.BlockSpec(memory_space=pltpu.VMEM)
    return _pcall(
        body, name="sum_small", in_specs=[vm], out_specs=vm, out_shape=jax.ShapeDtypeStruct((rows, width), F32),
        scratch_shapes=[pltpu.VMEM((8 * rows, width), F32), pltpu.SemaphoreType.DMA((7,)), pltpu.SemaphoreType.DMA((7,))],
    )(part)


def _row_tile(R, C, itemsize=4, budget=1 << 20):
    for t in (512, 256, 128, 64, 32, 16, 8):
        if R % t == 0 and t * C * itemsize <= budget:
            return t
    return R


def _add_halves(g, recv, c, *, name):
    _, R, C = g.shape
    half = R // 2
    t = _row_tile(half, C)
    nb = half // t

    def body(c_ref, g_ref, r_ref, o_ref):
        o_ref[...] = (g_ref[...].astype(F32) + r_ref[...].astype(F32)).astype(o_ref.dtype)

    grid_spec = pltpu.PrefetchScalarGridSpec(
        num_scalar_prefetch=1, grid=(4, nb),
        in_specs=[pl.BlockSpec((1, t, C), lambda k, i, cr: (k, cr[0] * nb + i, 0)),
                  pl.BlockSpec((1, t, C), lambda k, i, cr: (k, i, 0))],
        out_specs=pl.BlockSpec((1, t, C), lambda k, i, cr: (k, i, 0)))
    return _pcall(body, name=name, grid_spec=grid_spec, out_shape=jax.ShapeDtypeStruct((4, half, C), g.dtype),
                  compiler_params=_params("parallel", "parallel"))(c, g, recv)


def _add_owners(mine, recv, *, name):
    half, C = mine.shape
    t = _row_tile(half, C)

    def body(m_ref, r_ref, o_ref):
        o_ref[...] = ((m_ref[...].astype(F32) + r_ref[0].astype(F32)) + r_ref[1].astype(F32)) + r_ref[2].astype(F32)

    return _pcall(body, name=name, grid=(half // t,),
                  in_specs=[pl.BlockSpec((t, C), lambda i: (i, 0)), pl.BlockSpec((3, t, C), lambda i: (0, i, 0))],
                  out_specs=pl.BlockSpec((t, C), lambda i: (i, 0)), out_shape=jax.ShapeDtypeStruct((half, C), F32),
                  compiler_params=_params("parallel"))(mine, recv)


def _adamw(w, g, m, v, *, name):
    R, C = w.shape
    t = _row_tile(R, C)
    c1 = 1.0 - ADAM_B1 ** ADAM_STEP
    c2 = 1.0 - ADAM_B2 ** ADAM_STEP

    def body(w_ref, g_ref, m_ref, v_ref, d_ref, nm_ref, nv_ref):
        gv = g_ref[...]
        mn = ADAM_B1 * m_ref[...] + (1.0 - ADAM_B1) * gv
        vn = ADAM_B2 * v_ref[...] + (1.0 - ADAM_B2) * (gv * gv)
        d_ref[...] = -ADAM_LR * ((mn / c1) / (jnp.sqrt(vn / c2) + ADAM_EPS) + ADAM_WD * w_ref[...])
        nm_ref[...] = mn
        nv_ref[...] = vn

    blk = pl.BlockSpec((t, C), lambda i: (i, 0))
    shp = jax.ShapeDtypeStruct((R, C), F32)
    return _pcall(body, name=name, grid=(R // t,), in_specs=[blk] * 4, out_specs=[blk] * 3, out_shape=[shp] * 3,
                  compiler_params=_params("parallel"))(w, g, m, v)


BIG = ("w_in", "w_dil_out", "w_fox_out", "w_out", "w_ffn_in", "w_ffn_down")
SMALL = ("norm_mix_g", "b_fgt", "b_gate", "norm_ffn_g", "norm_final_g")
ORDER = ("norm_mix_g", "w_in", "b_fgt", "b_gate", "w_dil_out", "w_fox_out", "w_out", "norm_ffn_g", "w_ffn_in",
         "w_ffn_down", "norm_final_g")
SMALL_ROWS = {"norm_mix_g": (0, 1), "b_gate": (1, 3), "norm_ffn_g": (3, 4), "norm_final_g": (4, 5), "b_fgt": (5, 6)}


def _columns_to_blocks(full, ncol):
    K = full.shape[0]
    return full.reshape(K, 4, ncol).transpose(1, 0, 2)


def _blocks_to_columns(blocks):
    n, K, ncol = blocks.shape
    return blocks.transpose(1, 0, 2).reshape(K, n * ncol)


def kernel(x, norm_mix_g, w_in, b_fgt, b_gate, w_dil_out, w_fox_out, w_out, norm_ffn_g, w_ffn_in, w_ffn_down, norm_final_g, loss_target, m_norm_mix_g, m_w_in, m_b_fgt, m_b_gate, m_w_dil_out, m_w_fox_out, m_w_out, m_norm_ffn_g, m_w_ffn_in, m_w_ffn_down, m_norm_final_g, v_norm_mix_g, v_w_in, v_b_fgt, v_b_gate, v_w_dil_out, v_w_fox_out, v_w_out, v_norm_ffn_g, v_w_ffn_in, v_w_ffn_down, v_norm_final_g):
    weights = dict(norm_mix_g=norm_mix_g, w_in=w_in, b_fgt=b_fgt, b_gate=b_gate, w_dil_out=w_dil_out,
                   w_fox_out=w_fox_out, w_out=w_out, norm_ffn_g=norm_ffn_g, w_ffn_in=w_ffn_in, w_ffn_down=w_ffn_down,
                   norm_final_g=norm_final_g)
    m_in = dict(norm_mix_g=m_norm_mix_g, w_in=m_w_in, b_fgt=m_b_fgt, b_gate=m_b_gate, w_dil_out=m_w_dil_out,
                w_fox_out=m_w_fox_out, w_out=m_w_out, norm_ffn_g=m_norm_ffn_g, w_ffn_in=m_w_ffn_in,
                w_ffn_down=m_w_ffn_down, norm_final_g=m_norm_final_g)
    v_in = dict(norm_mix_g=v_norm_mix_g, w_in=v_w_in, b_fgt=v_b_fgt, b_gate=v_b_gate, w_dil_out=v_w_dil_out,
                w_fox_out=v_w_fox_out, w_out=v_w_out, norm_ffn_g=v_norm_ffn_g, w_ffn_in=v_w_ffn_in,
                w_ffn_down=v_w_ffn_down, norm_final_g=v_norm_final_g)
    c = lax.axis_index("c")
    chip = 2 * lax.axis_index("x") + lax.axis_index("y")

    shards = {n: weights[n][0].astype(_CD) for n in BIG}
    in_shape = jax.ShapeDtypeStruct((4,) + shards["w_in"].shape, _CD)
    send_i, recv_i, in_src, in_land, token_in = _split_copy_start(
        [shards["w_in"]], [in_shape], _gather_copies, norm_mix_g, name="gather_in_start")
    late = BIG[1:]
    send_g, recv_g, late_src, late_land, token = _split_copy_start(
        [shards[n] for n in late], [jax.ShapeDtypeStruct((4,) + shards[n].shape, _CD) for n in late],
        _gather_copies, token_in, name="gather_late_start")
    adam_in = [t[0] + token_in[0, 0] for t in (w_in, m_w_in, v_w_in)]
    p = dict(norm_mix_g=norm_mix_g, b_fgt=jnp.pad(b_fgt, ((0, 0), (0, F_PAD - N_FOX_HEADS))), b_gate=b_gate,
             norm_ffn_g=norm_ffn_g, norm_final_g=norm_final_g.reshape(1, D_MODEL))

    def first_weights(after):
        own, lands = _split_copy_wait(send_i, recv_i, in_src, in_land, _gather_copies, [after] + adam_in,
                                      name="gather_in_wait")
        (g_in,) = _forward_halves(lands, name="gather_in_forward")
        full_in = _blocks_to_columns(lax.dynamic_update_index_in_dim(g_in, own[0], chip, 0))
        o3 = QKV_COLS
        o4 = o3 + N_FOX_HEADS
        return dict(qkv=full_in[:, :o3], f=jnp.pad(full_in[:, o3:o4], ((0, 0), (0, F_PAD - N_FOX_HEADS))),
                    g=full_in[:, o4:])

    def late_weights(after):
        own, lands = _split_copy_wait(send_g, recv_g, late_src, late_land, _gather_copies, after,
                                      name="gather_late_wait")
        lands = _forward_halves(lands, name="gather_late_forward")
        g_dil, g_fox, g_out, g_ffn_in, g_ffn_down = [
            lax.dynamic_update_index_in_dim(l, s, chip, 0) for l, s in zip(lands, own)]
        return dict(dil_out=_blocks_to_columns(g_dil), fox_out=_blocks_to_columns(g_fox),
                    out=g_out.reshape(D_MODEL, D_MODEL), ffn_in=g_ffn_in,
                    ffn_down=g_ffn_down.reshape(D_FF, D_MODEL))

    c_arr = jnp.reshape(c, (1,)).astype(jnp.int32)

    def to_blocks(n, full):
        shape = weights[n].shape
        if full.ndim == 3:
            return full
        if n in ("w_out", "w_ffn_down"):
            return full.reshape(4, shape[1], shape[2])
        return _columns_to_blocks(full, shape[2])

    def pair_sums(group, named):
        names = list(named)
        blocks = [to_blocks(n, named[n]) for n in names]
        from_sibling = _swap_halves(blocks, name=f"swap_halves_{group}")
        return [_add_halves(b, r, c_arr, name=f"add_halves_{n}") for b, r, n in zip(blocks, from_sibling, names)]

    in_flight = {}

    def grad_sink(group, gw):
        if group == "in":
            named = {"w_in": jnp.concatenate([gw["qkv"], gw["f"][:, :N_FOX_HEADS], gw["g"]], axis=1)}
        else:
            named = {"w_" + k: v for k, v in gw.items()}
        sums = pair_sums(group, named)
        started = _split_copy_start(sums, [jax.ShapeDtypeStruct((3,) + s.shape[1:], s.dtype) for s in sums],
                                    _scatter_copies, next(iter(gw.values())), name=f"scatter_{group}_start")
        in_flight[group] = (list(named), started)
        return started[-1]

    loss_part, grad_x, gw, small = _layer_step(x[0], loss_target[0], {}, p, late_weights, grad_sink,
                                               (token_in, token), first_weights)

    def owner_sums(names, sums, from_chips):
        return {n: _add_owners(lax.dynamic_index_in_dim(s, chip, 0, keepdims=False), r, name=f"add_owners_{n}")
                for n, s, r in zip(names, sums, from_chips)}

    halves = {}
    for group, (names, (send_s, recv_s, srcs, lands, _)) in in_flight.items():
        sums, from_chips = _split_copy_wait(send_s, recv_s, srcs, lands, _scatter_copies, grad_x,
                                            name=f"scatter_{group}_wait")
        halves.update(owner_sums(names, sums, from_chips))
    halves = [halves[n] for n in BIG]
    grads = {}
    for n, own, other in zip(BIG, halves, _share_halves(halves)):
        pair = jnp.stack([own, other])
        grads[n] = jnp.where(c == 0, pair, pair[::-1]).reshape(2 * own.shape[0], own.shape[1])

    packed = jnp.concatenate([
        small["norm_mix_g"], small["b_gate"].reshape(2, D_MODEL), small["norm_ffn_g"], small["norm_final_g"],
        jnp.pad(small["b_fgt"], ((0, 0), (0, D_MODEL - F_PAD))), jnp.zeros((2, D_MODEL), F32)], axis=0)
    summed = _sum_small(packed)
    for n in SMALL:
        lo, hi = SMALL_ROWS[n]
        grads[n] = summed[lo:hi].reshape(1, -1)[:, :weights[n].size]

    loss = lax.psum(loss_part[0, 0], ("x", "y", "c"))

    out_g, out_d, out_m, out_v = {}, {}, {}, {}
    for n in ORDER:
        shape = weights[n].shape
        two_d = shape[1:] if len(shape) == 3 else (1, weights[n].size)
        g2 = grads[n].reshape(two_d)
        wmv = adam_in if n == "w_in" else [t.reshape(two_d) for t in (weights[n], m_in[n], v_in[n])]
        d2, m2, v2 = _adamw(wmv[0], g2, wmv[1], wmv[2], name=f"adamw_{n}")
        out_g[n], out_d[n], out_m[n], out_v[n] = (g2.reshape(shape), d2.reshape(shape), m2.reshape(shape),
                                                  v2.reshape(shape))
    return (loss, grad_x[None], *[out_g[n] for n in ORDER], *[out_d[n] for n in ORDER],
            *[out_m[n] for n in ORDER], *[out_v[n] for n in ORDER])
```

```python
import numpy as np
import jax
import jax.numpy as jnp
from jax import lax
from jax.experimental import pallas as pl
from jax.experimental.pallas import tpu as pltpu

F32 = jnp.float32
_CD = jnp.bfloat16

D_MODEL = 1024
HEAD_DIM = 64
DIL_PAIRS = ((128, 1), (512, 4), (2048, 16))
N_DIL_GROUPS = 3
DIL_HEADS = 4
DIL_W = 128
DIL_OUT = DIL_HEADS * HEAD_DIM
DIL_WIDTH = N_DIL_GROUPS * DIL_OUT
N_FOX_HEADS = 8
FOX_WIDTH = N_FOX_HEADS * HEAD_DIM
D_FF = 2816
QKV_COLS = 3 * DIL_WIDTH + 3 * FOX_WIDTH
F_PAD = 128
RMS_EPS = 1e-6
NEG_INF = -1e30
ATTN_SCALE = HEAD_DIM ** -0.5
ADAM_LR, ADAM_B1, ADAM_B2, ADAM_EPS, ADAM_WD, ADAM_STEP = 0.001, 0.9, 0.999, 1e-08, 0.01, 10

VMEM_LIMIT = 48 * 1024 * 1024
MESH = pl.DeviceIdType.MESH
HBM_SPEC = pl.BlockSpec(memory_space=pltpu.HBM)


def _pcall(body, after=None, **kw):
    if after is None:
        return pl.pallas_call(body, **kw)
    n_in = len(kw["in_specs"])
    kw["in_specs"] = list(kw["in_specs"]) + [pl.BlockSpec(memory_space=pl.ANY)]

    def tied(*refs):
        return body(*refs[:n_in], *refs[n_in + 1:])

    call = pl.pallas_call(tied, **kw)
    return lambda *args: call(*args, after)


def _params(*sem):
    return pltpu.CompilerParams(dimension_semantics=sem, vmem_limit_bytes=VMEM_LIMIT)


def _pick(dim, pref):
    t = (min(pref, dim) // 128) * 128
    while t >= 128:
        if dim % t == 0:
            return t
        t -= 128
    return dim


def _mm(a, b, *, name, ta=False, tb=False, out_dtype=F32, add=None, tm=1024, tn=512, tk=2048, after=None,
        b_blocks=False, out_blocks=None):
    if ta:
        K, M = a.shape
    else:
        M, K = a.shape
    b_rows, b_cols = (b.shape[1], b.shape[0] * b.shape[2]) if b_blocks else b.shape
    if tb:
        N, K2 = b_rows, b_cols
    else:
        K2, N = b_rows, b_cols
    assert K == K2, (a.shape, b.shape)
    shard = b.shape[2] if b_blocks else None
    tm = _pick(M, tm)
    tn = _pick(shard if (b_blocks and not tb) else (out_blocks or N), tn)
    tk = _pick(shard if (b_blocks and tb) else K, tk)
    nk = K // tk
    dn = (((0 if ta else 1,), (1 if tb else 0,)), ((), ()))
    has_add = add is not None
    assert not (has_add and out_blocks)

    def body(*refs):
        a_ref, b_ref = refs[0], refs[1]
        add_ref = refs[2] if has_add else None
        o_ref = refs[3] if has_add else refs[2]
        bv = b_ref[0] if b_blocks else b_ref[...]
        p = lax.dot_general(a_ref[...].astype(_CD), bv.astype(_CD), dn, preferred_element_type=F32)

        def finish(r):
            if has_add:
                r = r + add_ref[...]
            if out_blocks:
                o_ref[0] = r.astype(out_dtype)
            else:
                o_ref[...] = r.astype(out_dtype)

        if nk == 1:
            finish(p)
        else:
            acc_ref = refs[-1]
            k = pl.program_id(2)

            @pl.when(k == 0)
            def _():
                acc_ref[...] = p

            @pl.when(k > 0)
            def _():
                acc_ref[...] += p

            @pl.when(k == nk - 1)
            def _():
                finish(acc_ref[...])

    a_spec = pl.BlockSpec((tk, tm), lambda i, j, k: (k, i)) if ta else pl.BlockSpec((tm, tk), lambda i, j, k: (i, k))
    if b_blocks and tb:
        per = shard // tk
        b_spec = pl.BlockSpec((1, tn, tk), lambda i, j, k: (k // per, j, k % per))
    elif b_blocks:
        per = shard // tn
        b_spec = pl.BlockSpec((1, tk, tn), lambda i, j, k: (j // per, k, j % per))
    else:
        b_spec = pl.BlockSpec((tn, tk), lambda i, j, k: (j, k)) if tb else pl.BlockSpec((tk, tn), lambda i, j, k: (k, j))
    if out_blocks:
        oper = out_blocks // tn
        o_spec = pl.BlockSpec((1, tm, tn), lambda i, j, k: (j // oper, i, j % oper))
        out_shape = jax.ShapeDtypeStruct((N // out_blocks, M, out_blocks), out_dtype)
    else:
        o_spec = pl.BlockSpec((tm, tn), lambda i, j, k: (i, j))
        out_shape = jax.ShapeDtypeStruct((M, N), out_dtype)
    in_specs = [a_spec, b_spec] + ([o_spec] if has_add else [])
    args = (a, b) + ((add,) if has_add else ())
    return _pcall(
        body, after, name=name, grid=(M // tm, N // tn, nk), in_specs=in_specs, out_specs=o_spec,
        out_shape=out_shape,
        scratch_shapes=[pltpu.VMEM((tm, tn), F32)] if nk > 1 else [],
        compiler_params=_params("parallel", "parallel", "arbitrary"),
    )(*args)


def _rms_fwd(x, g, *, name, tm=512, after=None):
    S, D = x.shape

    def body(x_ref, g_ref, h_ref):
        xv = x_ref[...]
        r = lax.rsqrt(jnp.mean(xv * xv, axis=-1, keepdims=True) + RMS_EPS)
        h_ref[...] = ((xv * r) * g_ref[...]).astype(h_ref.dtype)

    row = pl.BlockSpec((tm, D), lambda i: (i, 0))
    return _pcall(body, after, name=name, grid=(S // tm,), in_specs=[row, pl.BlockSpec((1, D), lambda i: (0, 0))],
                  out_specs=row, out_shape=jax.ShapeDtypeStruct((S, D), _CD), compiler_params=_params("parallel"))(x, g)


def _rms_bwd(x, g, dh, dres, *, name, tm=512, after=None):
    S, D = x.shape

    def body(x_ref, g_ref, dh_ref, dres_ref, dx_ref, dg_ref):
        xv = x_ref[...]
        r = lax.rsqrt(jnp.mean(xv * xv, axis=-1, keepdims=True) + RMS_EPS)
        xh = xv * r
        dhv = dh_ref[...]
        dxh = dhv * g_ref[...]
        dx_ref[...] = dres_ref[...] + r * (dxh - xh * jnp.mean(dxh * xh, axis=-1, keepdims=True))
        part = jnp.sum(dhv * xh, axis=0, keepdims=True)

        @pl.when(pl.program_id(0) == 0)
        def _():
            dg_ref[...] = part

        @pl.when(pl.program_id(0) > 0)
        def _():
            dg_ref[...] += part

    row = pl.BlockSpec((tm, D), lambda i: (i, 0))
    vec = pl.BlockSpec((1, D), lambda i: (0, 0))
    return _pcall(body, after, name=name, grid=(S // tm,), in_specs=[row, vec, row, row], out_specs=[row, vec],
                  out_shape=[jax.ShapeDtypeStruct((S, D), F32), jax.ShapeDtypeStruct((1, D), F32)],
                  compiler_params=_params("arbitrary"))(x, g, dh, dres)


def _loss_head(x, g, tgt, *, name, tm=512):
    S, D = x.shape

    def body(x_ref, g_ref, t_ref, loss_ref, dx_ref, dg_ref):
        xv = x_ref[...]
        gv = g_ref[...]
        r = lax.rsqrt(jnp.mean(xv * xv, axis=-1, keepdims=True) + RMS_EPS)
        xh = xv * r
        err = xh * gv - t_ref[...]
        lpart = 0.5 * jnp.sum(jnp.mean(err * err, axis=-1, keepdims=True), axis=0, keepdims=True)
        dy = err * (1.0 / D)
        dxh = dy * gv
        dx_ref[...] = r * (dxh - xh * jnp.mean(dxh * xh, axis=-1, keepdims=True))
        gpart = jnp.sum(dy * xh, axis=0, keepdims=True)

        @pl.when(pl.program_id(0) == 0)
        def _():
            loss_ref[...] = lpart
            dg_ref[...] = gpart

        @pl.when(pl.program_id(0) > 0)
        def _():
            loss_ref[...] += lpart
            dg_ref[...] += gpart

    row = pl.BlockSpec((tm, D), lambda i: (i, 0))
    vec = pl.BlockSpec((1, D), lambda i: (0, 0))
    one = pl.BlockSpec((1, 1), lambda i: (0, 0))
    return _pcall(body, name=name, grid=(S // tm,), in_specs=[row, vec, row], out_specs=[one, row, vec],
                  out_shape=[jax.ShapeDtypeStruct((1, 1), F32), jax.ShapeDtypeStruct((S, D), F32),
                             jax.ShapeDtypeStruct((1, D), F32)],
                  compiler_params=_params("arbitrary"))(x, g, tgt)


def _sigmoid(z):
    return 1.0 / (1.0 + jnp.exp(-z))


def _gate_fwd(gl, bg, ya, yb, *, name, tm=512):
    S, D = ya.shape

    def body(za_ref, zb_ref, ba_ref, bb_ref, ya_ref, yb_ref, o_ref):
        ga = _sigmoid(za_ref[...].astype(F32) + ba_ref[...])
        gb = _sigmoid(zb_ref[...].astype(F32) + bb_ref[...])
        o_ref[...] = (ga * ya_ref[...].astype(F32) + gb * yb_ref[...].astype(F32)).astype(o_ref.dtype)

    lo = pl.BlockSpec((tm, D), lambda i: (i, 0))
    hi = pl.BlockSpec((tm, D), lambda i: (i, 1))
    vlo = pl.BlockSpec((1, D), lambda i: (0, 0))
    vhi = pl.BlockSpec((1, D), lambda i: (0, 1))
    return _pcall(body, name=name, grid=(S // tm,), in_specs=[lo, hi, vlo, vhi, lo, lo], out_specs=lo,
                  out_shape=jax.ShapeDtypeStruct((S, D), _CD), compiler_params=_params("parallel"))(gl, gl, bg, bg, ya, yb)


def _gate_bwd(dm, gl, bg, ya, yb, *, name, tm=256):
    S, D = ya.shape

    def body(dm_ref, za_ref, zb_ref, ba_ref, bb_ref, ya_ref, yb_ref, dya_ref, dyb_ref, dgl_ref, dbg_ref):
        dmv = dm_ref[...].astype(F32)
        ga = _sigmoid(za_ref[...].astype(F32) + ba_ref[...])
        gb = _sigmoid(zb_ref[...].astype(F32) + bb_ref[...])
        dya_ref[...] = (dmv * ga).astype(dya_ref.dtype)
        dyb_ref[...] = (dmv * gb).astype(dyb_ref.dtype)
        dza = dmv * ya_ref[...].astype(F32) * ga * (1.0 - ga)
        dzb = dmv * yb_ref[...].astype(F32) * gb * (1.0 - gb)
        dgl_ref[:, :D] = dza.astype(dgl_ref.dtype)
        dgl_ref[:, D:] = dzb.astype(dgl_ref.dtype)
        pa = jnp.sum(dza, axis=0, keepdims=True)
        pb = jnp.sum(dzb, axis=0, keepdims=True)

        @pl.when(pl.program_id(0) == 0)
        def _():
            dbg_ref[:, :D] = pa
            dbg_ref[:, D:] = pb

        @pl.when(pl.program_id(0) > 0)
        def _():
            dbg_ref[:, :D] += pa
            dbg_ref[:, D:] += pb

    lo = pl.BlockSpec((tm, D), lambda i: (i, 0))
    hi = pl.BlockSpec((tm, D), lambda i: (i, 1))
    vlo = pl.BlockSpec((1, D), lambda i: (0, 0))
    vhi = pl.BlockSpec((1, D), lambda i: (0, 1))
    wide = pl.BlockSpec((tm, 2 * D), lambda i: (i, 0))
    vwide = pl.BlockSpec((1, 2 * D), lambda i: (0, 0))
    return _pcall(body, name=name, grid=(S // tm,), in_specs=[lo, lo, hi, vlo, vhi, lo, lo],
                  out_specs=[lo, lo, wide, vwide],
                  out_shape=[jax.ShapeDtypeStruct((S, D), _CD), jax.ShapeDtypeStruct((S, D), _CD),
                             jax.ShapeDtypeStruct((S, 2 * D), _CD), jax.ShapeDtypeStruct((1, 2 * D), F32)],
                  compiler_params=_params("arbitrary"))(dm, gl, gl, bg, bg, ya, yb)


def _swiglu_fwd(gu, *, name, tm=256):
    S, F2 = gu.shape
    F = F2 // 2

    def body(g_ref, u_ref, o_ref):
        gv = g_ref[...].astype(F32)
        o_ref[...] = (gv * _sigmoid(gv) * u_ref[...].astype(F32)).astype(o_ref.dtype)

    lo = pl.BlockSpec((tm, F), lambda i: (i, 0))
    hi = pl.BlockSpec((tm, F), lambda i: (i, 1))
    return _pcall(body, name=name, grid=(S // tm,), in_specs=[lo, hi], out_specs=lo,
                  out_shape=jax.ShapeDtypeStruct((S, F), _CD), compiler_params=_params("parallel"))(gu, gu)


def _swiglu_bwd(dact, gu, *, name, tm=256):
    S, F2 = gu.shape
    F = F2 // 2

    def body(d_ref, g_ref, u_ref, o_ref):
        dv = d_ref[...].astype(F32)
        gv = g_ref[...].astype(F32)
        sg = _sigmoid(gv)
        o_ref[:, :F] = (dv * u_ref[...].astype(F32) * (sg * (1.0 + gv * (1.0 - sg)))).astype(o_ref.dtype)
        o_ref[:, F:] = (dv * (gv * sg)).astype(o_ref.dtype)

    lo = pl.BlockSpec((tm, F), lambda i: (i, 0))
    hi = pl.BlockSpec((tm, F), lambda i: (i, 1))
    return _pcall(body, name=name, grid=(S // tm,), in_specs=[lo, lo, hi],
                  out_specs=pl.BlockSpec((tm, F2), lambda i: (i, 0)),
                  out_shape=jax.ShapeDtypeStruct((S, F2), _CD), compiler_params=_params("parallel"))(dact, gu, gu)


def _split3(x):
    hi = x.astype(jnp.bfloat16)
    r1 = x - hi.astype(F32)
    mid = r1.astype(jnp.bfloat16)
    lo = (r1 - mid.astype(F32)).astype(jnp.bfloat16)
    return hi, mid, lo


def _ones_dot_left(ones, x):
    return sum(jnp.dot(ones, p, preferred_element_type=F32) for p in _split3(x))


def _ones_dot_right(x, ones):
    return sum(jnp.dot(p, ones, preferred_element_type=F32) for p in _split3(x))


def _head_sum(x):
    n = x.shape[1]
    r = lax.broadcasted_iota(jnp.int32, (n, n), 0) // HEAD_DIM
    c = lax.broadcasted_iota(jnp.int32, (n, n), 1) // HEAD_DIM
    return _ones_dot_right(x, (r == c).astype(jnp.bfloat16))


def _log_sigmoid(z):
    e = jnp.exp(-jnp.abs(z))
    t = 1.0 + e
    log1p_e = jnp.where(t == 1.0, e, jnp.log(t) * (e / jnp.where(t == 1.0, 1.0, t - 1.0)))
    return jnp.minimum(z, 0.0) - log1p_e


def _fox_cumsum(zf, bf, *, name):
    S, W = zf.shape
    nb = S // 128

    def body(z_ref, b_ref, c_ref):
        tri = (lax.broadcasted_iota(jnp.int32, (128, 128), 0) >= lax.broadcasted_iota(jnp.int32, (128, 128), 1))
        tri = tri.astype(jnp.bfloat16)

        def step(i, carry):
            rows = pl.ds(pl.multiple_of(i * 128, 128), 128)
            lf = _log_sigmoid(z_ref[rows, :] + b_ref[...])
            cb = _ones_dot_left(tri, lf) + carry
            c_ref[rows, :] = cb
            return cb[127:128, :]

        lax.fori_loop(0, nb, step, jnp.zeros((1, W), F32))

    return _pcall(body, name=name, out_shape=jax.ShapeDtypeStruct((S, W), F32),
                  compiler_params=pltpu.CompilerParams(vmem_limit_bytes=VMEM_LIMIT))(zf, bf)


def _fox_cumsum_bwd(dc, zf, bf, *, name):
    S, W = zf.shape
    nb = S // 128

    def body(dc_ref, z_ref, b_ref, dz_ref, db_ref):
        tri = (lax.broadcasted_iota(jnp.int32, (128, 128), 0) <= lax.broadcasted_iota(jnp.int32, (128, 128), 1))
        tri = tri.astype(jnp.bfloat16)

        def step(k, carry):
            tail, acc = carry
            i = nb - 1 - k
            rows = pl.ds(pl.multiple_of(i * 128, 128), 128)
            dlf = _ones_dot_left(tri, dc_ref[rows, :]) + tail
            dz = dlf * _sigmoid(-(z_ref[rows, :] + b_ref[...]))
            dz_ref[rows, :] = dz
            return dlf[0:1, :], acc + jnp.sum(dz, axis=0, keepdims=True)

        _, acc = lax.fori_loop(0, nb, step, (jnp.zeros((1, W), F32), jnp.zeros((1, W), F32)))
        db_ref[...] = acc

    return _pcall(body, name=name,
                  out_shape=[jax.ShapeDtypeStruct((S, W), F32), jax.ShapeDtypeStruct((1, W), F32)],
                  compiler_params=pltpu.CompilerParams(vmem_limit_bytes=VMEM_LIMIT))(dc, zf, bf)


def _dil_slopes(group):
    h = np.arange(1, N_DIL_GROUPS * DIL_HEADS + 1, dtype=np.float32)
    s = (np.float32(2.0) ** (np.float32(-8.0) * h / np.float32(N_DIL_GROUPS * DIL_HEADS))).astype(np.float32)
    return [float(v) for v in s.reshape(N_DIL_GROUPS, DIL_HEADS)[group]]


def _dil_tiles(i, n, blocks_per_seq):
    qi = lax.broadcasted_iota(jnp.int32, (DIL_W, DIL_W), 0)
    kj = lax.broadcasted_iota(jnp.int32, (DIL_W, DIL_W), 1)
    first = ((4 * n + i) % blocks_per_seq) == 0
    valid_prev = jnp.logical_and(kj >= qi, jnp.logical_not(first))
    valid_cur = kj <= qi
    rel_prev = (qi - kj + DIL_W).astype(F32)
    rel_cur = (qi - kj).astype(F32)
    return valid_prev, valid_cur, rel_prev, rel_cur


CHUNK = 4 * DIL_W


def _dil_fwd(q, k, v, group, *, name):
    S = q.shape[0]
    dilation = DIL_PAIRS[group][1]
    bps = (S // dilation) // DIL_W
    slopes = _dil_slopes(group)
    nt = (((1,), (1,)), ((), ()))

    def body(q_ref, k_ref, v_ref, kp_ref, vp_ref, o_ref, l_ref):
        n = pl.program_id(0)
        for i in range(4):
            valid_prev, valid_cur, rel_prev, rel_cur = _dil_tiles(i, n, bps)
            rows = slice(i * DIL_W, (i + 1) * DIL_W)
            prow = slice((i - 1) * DIL_W, i * DIL_W)
            for h in range(DIL_HEADS):
                cols = slice(h * HEAD_DIM, (h + 1) * HEAD_DIM)
                qh = q_ref[rows, cols]
                kc, vc = k_ref[rows, cols], v_ref[rows, cols]
                kp = kp_ref[:, cols] if i == 0 else k_ref[prow, cols]
                vp = vp_ref[:, cols] if i == 0 else v_ref[prow, cols]
                sl = slopes[h] * dilation
                sp = lax.dot_general(qh, kp, nt, preferred_element_type=F32) * ATTN_SCALE - sl * rel_prev
                sc = lax.dot_general(qh, kc, nt, preferred_element_type=F32) * ATTN_SCALE - sl * rel_cur
                sp = jnp.where(valid_prev, sp, NEG_INF)
                sc = jnp.where(valid_cur, sc, NEG_INF)
                m = jnp.maximum(jnp.max(sp, axis=-1, keepdims=True), jnp.max(sc, axis=-1, keepdims=True))
                pp, pc = jnp.exp(sp - m), jnp.exp(sc - m)
                den = jnp.sum(pp, axis=-1, keepdims=True) + jnp.sum(pc, axis=-1, keepdims=True)
                acc = (jnp.dot(pp.astype(_CD), vp, preferred_element_type=F32)
                       + jnp.dot(pc.astype(_CD), vc, preferred_element_type=F32))
                o_ref[rows, cols] = acc / den
                l_ref[rows, cols] = jnp.broadcast_to(m + jnp.log(den), (DIL_W, HEAD_DIM))

    cur = pl.BlockSpec((CHUNK, DIL_OUT), lambda n: (n, 0))
    prev = pl.BlockSpec((DIL_W, DIL_OUT), lambda n: (jnp.maximum(4 * n - 1, 0), 0))
    return _pcall(body, name=name, grid=(S // CHUNK,), in_specs=[cur, cur, cur, prev, prev], out_specs=[cur, cur],
                  out_shape=[jax.ShapeDtypeStruct((S, DIL_OUT), F32), jax.ShapeDtypeStruct((S, DIL_OUT), F32)],
                  compiler_params=_params("parallel"))(q, k, v, k, v)


def _dil_bwd(q, k, v, o, lse, do, dlse, group, *, name):
    S = q.shape[0]
    dilation = DIL_PAIRS[group][1]
    bps = (S // dilation) // DIL_W
    slopes = _dil_slopes(group)
    nchunk = S // CHUNK
    nt = (((1,), (1,)), ((), ()))
    tn = (((0,), (0,)), ((), ()))

    def body(q_ref, k_ref, v_ref, kp_ref, vp_ref, o_ref, l_ref, do_ref, dl_ref, dq_ref, dk_ref, dv_ref,
             dk_s, dv_s):
        step = pl.program_id(0)
        n = nchunk - 1 - step

        @pl.when(step == 0)
        def _():
            dk_s[CHUNK:, :] = jnp.zeros((DIL_W, DIL_OUT), F32)
            dv_s[CHUNK:, :] = jnp.zeros((DIL_W, DIL_OUT), F32)

        dk_s[:CHUNK, :] = jnp.zeros((CHUNK, DIL_OUT), F32)
        dv_s[:CHUNK, :] = jnp.zeros((CHUNK, DIL_OUT), F32)
        for i in range(4):
            valid_prev, valid_cur, rel_prev, rel_cur = _dil_tiles(i, n, bps)
            rows = slice(i * DIL_W, (i + 1) * DIL_W)
            prow = slice((i - 1) * DIL_W, i * DIL_W)
            s_prev = slice(i * DIL_W, (i + 1) * DIL_W)
            s_cur = slice((i + 1) * DIL_W, (i + 2) * DIL_W)
            for h in range(DIL_HEADS):
                cols = slice(h * HEAD_DIM, (h + 1) * HEAD_DIM)
                qh = q_ref[rows, cols]
                kc, vc = k_ref[rows, cols], v_ref[rows, cols]
                kp = kp_ref[:, cols] if i == 0 else k_ref[prow, cols]
                vp = vp_ref[:, cols] if i == 0 else v_ref[prow, cols]
                sl = slopes[h] * dilation
                lh = l_ref[rows, h * HEAD_DIM:h * HEAD_DIM + 1]
                sp = lax.dot_general(qh, kp, nt, preferred_element_type=F32) * ATTN_SCALE - sl * rel_prev
                sc = lax.dot_general(qh, kc, nt, preferred_element_type=F32) * ATTN_SCALE - sl * rel_cur
                pp = jnp.exp(jnp.where(valid_prev, sp, NEG_INF) - lh)
                pc = jnp.exp(jnp.where(valid_cur, sc, NEG_INF) - lh)
                doh = do_ref[rows, cols]
                dsum = jnp.sum(doh * o_ref[rows, cols], axis=-1, keepdims=True)
                shift = dl_ref[rows, h * HEAD_DIM:h * HEAD_DIM + 1] - dsum
                dob = doh.astype(_CD)
                dsp = pp * (lax.dot_general(dob, vp, nt, preferred_element_type=F32) + shift)
                dsc = pc * (lax.dot_general(dob, vc, nt, preferred_element_type=F32) + shift)
                dspb = (dsp * ATTN_SCALE).astype(_CD)
                dscb = (dsc * ATTN_SCALE).astype(_CD)
                dq_ref[rows, cols] = (jnp.dot(dspb, kp, preferred_element_type=F32)
                                      + jnp.dot(dscb, kc, preferred_element_type=F32)).astype(dq_ref.dtype)
                dk_s[s_prev, cols] += lax.dot_general(dspb, qh, tn, preferred_element_type=F32)
                dk_s[s_cur, cols] += lax.dot_general(dscb, qh, tn, preferred_element_type=F32)
                dv_s[s_prev, cols] += lax.dot_general(pp.astype(_CD), dob, tn, preferred_element_type=F32)
                dv_s[s_cur, cols] += lax.dot_general(pc.astype(_CD), dob, tn, preferred_element_type=F32)
        dk_ref[...] = dk_s[DIL_W:, :].astype(dk_ref.dtype)
        dv_ref[...] = dv_s[DIL_W:, :].astype(dv_ref.dtype)
        dk_s[CHUNK:, :] = dk_s[:DIL_W, :]
        dv_s[CHUNK:, :] = dv_s[:DIL_W, :]

    cur = pl.BlockSpec((CHUNK, DIL_OUT), lambda s: (nchunk - 1 - s, 0))
    prev = pl.BlockSpec((DIL_W, DIL_OUT), lambda s: (jnp.maximum(4 * (nchunk - 1 - s) - 1, 0), 0))
    shp = jax.ShapeDtypeStruct((S, DIL_OUT), _CD)
    return _pcall(body, name=name, grid=(nchunk,), in_specs=[cur, cur, cur, prev, prev, cur, cur, cur, cur],
                  out_specs=[cur, cur, cur], out_shape=[shp, shp, shp],
                  scratch_shapes=[pltpu.VMEM((CHUNK + DIL_W, DIL_OUT), F32), pltpu.VMEM((CHUNK + DIL_W, DIL_OUT), F32)],
                  compiler_params=_params("arbitrary"))(q, k, v, k, v, o, lse, do, dlse)


def _dil_mix_fwd(os_, ls_, *, name, tm=512):
    S, W = os_[0].shape

    def body(o0, o1, o2, l0, l1, l2, out_ref):
        ls = [l0[...], l1[...], l2[...]]
        m = jnp.maximum(jnp.maximum(ls[0], ls[1]), ls[2])
        es = [jnp.exp(l - m) for l in ls]
        den = es[0] + es[1] + es[2]
        out_ref[...] = ((es[0] * o0[...] + es[1] * o1[...] + es[2] * o2[...]) / den).astype(out_ref.dtype)

    row = pl.BlockSpec((tm, W), lambda i: (i, 0))
    return _pcall(body, name=name, grid=(S // tm,), in_specs=[row] * 6, out_specs=row,
                  out_shape=jax.ShapeDtypeStruct((S, W), _CD), compiler_params=_params("parallel"))(*os_, *ls_)


def _dil_mix_bwd(doa, os_, ls_, *, name, tm=512, after=None):
    S, W = doa.shape

    def body(d_ref, o0, o1, o2, l0, l1, l2, do0, do1, do2, dl0, dl1, dl2):
        dv = d_ref[...]
        ls = [l0[...], l1[...], l2[...]]
        m = jnp.maximum(jnp.maximum(ls[0], ls[1]), ls[2])
        es = [jnp.exp(l - m) for l in ls]
        den = es[0] + es[1] + es[2]
        al = [e / den for e in es]
        da = [_head_sum(dv * o[...]) for o in (o0, o1, o2)]
        mean = al[0] * da[0] + al[1] * da[1] + al[2] * da[2]
        for a, d_, do_ref, dl_ref in zip(al, da, (do0, do1, do2), (dl0, dl1, dl2)):
            do_ref[...] = a * dv
            dl_ref[...] = a * (d_ - mean)

    row = pl.BlockSpec((tm, W), lambda i: (i, 0))
    shp = jax.ShapeDtypeStruct((S, W), F32)
    return _pcall(body, after, name=name, grid=(S // tm,), in_specs=[row] * 7, out_specs=[row] * 6, out_shape=[shp] * 6,
                  compiler_params=_params("parallel"))(doa, *os_, *ls_)


FOX_T = 512


PACK = 2 * HEAD_DIM
HEAD_PAIRS = N_FOX_HEADS // 2
FOX_HPS = 4
Q_BLOCK0 = (3 * DIL_WIDTH) // PACK
K_BLOCK0 = (3 * DIL_WIDTH + FOX_WIDTH) // PACK
V_BLOCK0 = (3 * DIL_WIDTH + 2 * FOX_WIDTH) // PACK


def _pieces(x):
    hi = x.astype(jnp.bfloat16).astype(F32)
    r = x - hi
    mid = r.astype(jnp.bfloat16).astype(F32)
    lo = (r - mid).astype(jnp.bfloat16).astype(F32)
    return [hi, mid, lo]


def _extras(first, second, rows):
    lane = lax.broadcasted_iota(jnp.int32, (rows, HEAD_DIM), 1)
    out = jnp.zeros((rows, HEAD_DIM), F32)
    for idx, val in enumerate(list(first) + list(second)):
        out = jnp.where(lane == idx, val, out)
    return out


def _head_column(c, h):
    lane = lax.broadcasted_iota(jnp.int32, c.shape, 1)
    return jnp.sum(jnp.where(lane == h, c, 0.0), axis=1, keepdims=True)


ONES3 = [1.0, 1.0, 1.0]
ZEROS3 = [0.0, 0.0, 0.0]


def _fox_pack_fwd(qkv, c, *, name, tm=512):
    S = qkv.shape[0]

    def body(q_ref, k_ref, v_ref, c_ref, qo_ref, ko_ref, vo_ref):
        hp = pl.program_id(1)
        cv = c_ref[...]
        for hh in range(2):
            ch = _pieces(_head_column(cv, 2 * hp + hh))
            src = slice(hh * HEAD_DIM, (hh + 1) * HEAD_DIM)
            lo = slice(hh * PACK, hh * PACK + HEAD_DIM)
            hi = slice(hh * PACK + HEAD_DIM, (hh + 1) * PACK)
            qo_ref[:, lo] = (q_ref[:, src].astype(F32) * ATTN_SCALE).astype(qo_ref.dtype)
            qo_ref[:, hi] = _extras(ch, ONES3, tm).astype(qo_ref.dtype)
            ko_ref[:, lo] = k_ref[:, src]
            ko_ref[:, hi] = _extras(ONES3, [-p for p in ch], tm).astype(ko_ref.dtype)
            vo_ref[:, lo] = v_ref[:, src]
            vo_ref[:, hi] = _extras(ONES3, ZEROS3, tm).astype(vo_ref.dtype)

    def src(block0):
        return pl.BlockSpec((tm, PACK), lambda i, hp: (i, block0 + hp))

    out = pl.BlockSpec((tm, 2 * PACK), lambda i, hp: (i, hp))
    shp = jax.ShapeDtypeStruct((S, N_FOX_HEADS * PACK), _CD)
    return _pcall(body, name=name, grid=(S // tm, HEAD_PAIRS),
                  in_specs=[src(Q_BLOCK0), src(K_BLOCK0), src(V_BLOCK0), pl.BlockSpec((tm, PACK), lambda i, hp: (i, 0))],
                  out_specs=[out, out, out], out_shape=[shp, shp, shp],
                  compiler_params=_params("parallel", "parallel"))(qkv, qkv, qkv, c)


def _fox_fwd(qp, kp, vp, *, name):
    S = qp.shape[0]
    nt = S // FOX_T
    nt_dims = (((1,), (1,)), ((), ()))
    tn_dims = (((0,), (0,)), ((), ()))

    def body(i_tab, j_tab, q_ref, k_ref, v_ref, o_ref, l_ref, m_s, acc_s):
        t = pl.program_id(1)
        i, j = i_tab[t], j_tab[t]

        @pl.when(j == 0)
        def _():
            m_s[...] = jnp.full((FOX_HPS, 1, FOX_T), NEG_INF, F32)
            acc_s[...] = jnp.zeros((FOX_HPS, PACK, FOX_T), F32)

        def tile(diagonal):
            for hh in range(FOX_HPS):
                cols = slice(hh * PACK, (hh + 1) * PACK)
                st = lax.dot_general(k_ref[:, cols], q_ref[:, cols], nt_dims, preferred_element_type=F32)
                if diagonal:
                    key = lax.broadcasted_iota(jnp.int32, (FOX_T, FOX_T), 0)
                    qry = lax.broadcasted_iota(jnp.int32, (FOX_T, FOX_T), 1)
                    st = jnp.where(key <= qry, st, NEG_INF)
                m_old = m_s[hh]
                m_new = jnp.maximum(m_old, jnp.max(st, axis=0, keepdims=True))
                pt = jnp.exp(st - m_new)
                acc_s[hh] = jnp.exp(m_old - m_new) * acc_s[hh] + lax.dot_general(
                    v_ref[:, cols], pt.astype(_CD), tn_dims, preferred_element_type=F32)
                m_s[hh] = m_new

        @pl.when(j < i)
        def _():
            tile(False)

        @pl.when(j == i)
        def _():
            tile(True)
            for hh in range(FOX_HPS):
                acc = acc_s[hh]
                den = acc[HEAD_DIM:HEAD_DIM + 1, :]
                cols = slice(hh * HEAD_DIM, (hh + 1) * HEAD_DIM)
                o_ref[:, cols] = (acc[:HEAD_DIM, :] / den).T
                l_ref[:, cols] = jnp.broadcast_to(m_s[hh] + jnp.log(den), (HEAD_DIM, FOX_T)).T

    pairs = [(i, j) for i in range(nt) for j in range(i + 1)]
    i_tab = jnp.asarray([p[0] for p in pairs], jnp.int32)
    j_tab = jnp.asarray([p[1] for p in pairs], jnp.int32)
    qs = pl.BlockSpec((FOX_T, FOX_HPS * PACK), lambda hp, t, it, jt: (it[t], hp))
    ks = pl.BlockSpec((FOX_T, FOX_HPS * PACK), lambda hp, t, it, jt: (jt[t], hp))
    os_ = pl.BlockSpec((FOX_T, FOX_HPS * HEAD_DIM), lambda hp, t, it, jt: (it[t], hp))
    shp = jax.ShapeDtypeStruct((S, FOX_WIDTH), F32)
    grid_spec = pltpu.PrefetchScalarGridSpec(
        num_scalar_prefetch=2, grid=(N_FOX_HEADS // FOX_HPS, len(pairs)), in_specs=[qs, ks, ks], out_specs=[os_, os_],
        scratch_shapes=[pltpu.VMEM((FOX_HPS, 1, FOX_T), F32), pltpu.VMEM((FOX_HPS, PACK, FOX_T), F32)])
    return _pcall(body, name=name, grid_spec=grid_spec, out_shape=[shp, shp],
                  compiler_params=_params("parallel", "arbitrary"))(i_tab, j_tab, qp, kp, vp)


def _fox_pack_bwd(qkv, c, o, lse, do, *, name, tm=512, after=None):
    S = qkv.shape[0]

    def body(q_ref, c_ref, o_ref, l_ref, do_ref, qo_ref, do_out_ref):
        hp = pl.program_id(1)
        cv = c_ref[...]
        for hh in range(2):
            src = slice(hh * HEAD_DIM, (hh + 1) * HEAD_DIM)
            lo = slice(hh * PACK, hh * PACK + HEAD_DIM)
            hi = slice(hh * PACK + HEAD_DIM, (hh + 1) * PACK)
            shift = _head_column(cv, 2 * hp + hh) - l_ref[:, hh * HEAD_DIM:hh * HEAD_DIM + 1]
            dov = do_ref[:, src]
            dsum = jnp.sum(dov * o_ref[:, src], axis=-1, keepdims=True)
            qo_ref[:, lo] = (q_ref[:, src].astype(F32) * ATTN_SCALE).astype(qo_ref.dtype)
            qo_ref[:, hi] = _extras(_pieces(shift), ONES3, tm).astype(qo_ref.dtype)
            do_out_ref[:, lo] = dov.astype(do_out_ref.dtype)
            do_out_ref[:, hi] = _extras(_pieces(-dsum), ZEROS3, tm).astype(do_out_ref.dtype)

    pair = pl.BlockSpec((tm, PACK), lambda i, hp: (i, hp))
    out = pl.BlockSpec((tm, 2 * PACK), lambda i, hp: (i, hp))
    shp = jax.ShapeDtypeStruct((S, N_FOX_HEADS * PACK), _CD)
    return _pcall(body, after, name=name, grid=(S // tm, HEAD_PAIRS),
                  in_specs=[pl.BlockSpec((tm, PACK), lambda i, hp: (i, Q_BLOCK0 + hp)),
                            pl.BlockSpec((tm, PACK), lambda i, hp: (i, 0)), pair, pair, pair],
                  out_specs=[out, out], out_shape=[shp, shp],
                  compiler_params=_params("parallel", "parallel"))(qkv, c, o, lse, do)


def _fox_bwd(qp, kp, vp, dop, *, name):
    S = qp.shape[0]
    nt = S // FOX_T
    nt_dims = (((1,), (1,)), ((), ()))
    tn_dims = (((0,), (0,)), ((), ()))

    def body(i_tab, j_tab, q_ref, k_ref, v_ref, do_ref, dq_ref, dk_ref, dv_ref, dc_ref, dr_ref,
             dq_s, dk_s, dv_s, dc_s, dr_s):
        t = pl.program_id(1)
        i, j = i_tab[t], j_tab[t]

        @pl.when(t == 0)
        def _():
            dq_s[...] = jnp.zeros((S, FOX_HPS * PACK), F32)
            dr_s[...] = jnp.zeros((FOX_HPS, 1, S), F32)

        @pl.when(i == j)
        def _():
            dk_s[...] = jnp.zeros((FOX_T, FOX_HPS * PACK), F32)
            dv_s[...] = jnp.zeros((FOX_T, FOX_HPS * PACK), F32)
            dc_s[...] = jnp.zeros((FOX_HPS, FOX_T, 1), F32)

        def tile(diagonal):
            rows = pl.ds(pl.multiple_of(i * FOX_T, FOX_T), FOX_T)
            for hh in range(FOX_HPS):
                cols = slice(hh * PACK, (hh + 1) * PACK)
                qv, kv, vv, dov = q_ref[:, cols], k_ref[:, cols], v_ref[:, cols], do_ref[:, cols]
                pt = jnp.exp(lax.dot_general(kv, qv, nt_dims, preferred_element_type=F32))
                if diagonal:
                    key = lax.broadcasted_iota(jnp.int32, (FOX_T, FOX_T), 0)
                    qry = lax.broadcasted_iota(jnp.int32, (FOX_T, FOX_T), 1)
                    pt = jnp.where(key <= qry, pt, 0.0)
                dst = pt * lax.dot_general(vv, dov, nt_dims, preferred_element_type=F32)
                dsb = dst.astype(_CD)
                dc_s[hh] += jnp.sum(dst, axis=1, keepdims=True)
                dr_s[hh, :, rows] += jnp.sum(dst, axis=0, keepdims=True)
                dv_s[:, cols] += jnp.dot(pt.astype(_CD), dov, preferred_element_type=F32)
                dk_s[:, cols] += jnp.dot(dsb, qv, preferred_element_type=F32)
                dq_s[rows, cols] += lax.dot_general(dsb, kv, tn_dims, preferred_element_type=F32)

        @pl.when(i > j)
        def _():
            tile(False)

        @pl.when(i == j)
        def _():
            tile(True)

        @pl.when(i == nt - 1)
        def _():
            for hh in range(FOX_HPS):
                src = slice(hh * PACK, hh * PACK + HEAD_DIM)
                dst_cols = slice(hh * HEAD_DIM, (hh + 1) * HEAD_DIM)
                dk_ref[:, dst_cols] = dk_s[:, src].astype(dk_ref.dtype)
                dv_ref[:, dst_cols] = dv_s[:, src].astype(dv_ref.dtype)
                dc_ref[:, dst_cols] = jnp.broadcast_to(dc_s[hh], (FOX_T, HEAD_DIM))

        @pl.when(t == len(pairs) - 1)
        def _():
            for hh in range(FOX_HPS):
                dq_ref[:, hh * HEAD_DIM:(hh + 1) * HEAD_DIM] = (
                    dq_s[:, hh * PACK:hh * PACK + HEAD_DIM] * ATTN_SCALE).astype(dq_ref.dtype)
            dr_ref[...] = dr_s[...]

    pairs = [(i, j) for j in range(nt) for i in range(j, nt)]
    i_tab = jnp.asarray([p[0] for p in pairs], jnp.int32)
    j_tab = jnp.asarray([p[1] for p in pairs], jnp.int32)
    wide, narrow = FOX_HPS * PACK, FOX_HPS * HEAD_DIM
    qs = pl.BlockSpec((FOX_T, wide), lambda hp, t, it, jt: (it[t], hp))
    ks = pl.BlockSpec((FOX_T, wide), lambda hp, t, it, jt: (jt[t], hp))
    whole = pl.BlockSpec((S, narrow), lambda hp, t, it, jt: (0, hp))
    cs = pl.BlockSpec((FOX_T, narrow), lambda hp, t, it, jt: (jt[t], hp))
    rs = pl.BlockSpec((FOX_HPS, 1, S), lambda hp, t, it, jt: (hp, 0, 0))
    shp = jax.ShapeDtypeStruct((S, FOX_WIDTH), _CD)
    grid_spec = pltpu.PrefetchScalarGridSpec(
        num_scalar_prefetch=2, grid=(N_FOX_HEADS // FOX_HPS, len(pairs)), in_specs=[qs, ks, ks, qs],
        out_specs=[whole, cs, cs, cs, rs],
        scratch_shapes=[pltpu.VMEM((S, wide), F32), pltpu.VMEM((FOX_T, wide), F32),
                        pltpu.VMEM((FOX_T, wide), F32), pltpu.VMEM((FOX_HPS, FOX_T, 1), F32),
                        pltpu.VMEM((FOX_HPS, 1, S), F32)])
    return _pcall(body, name=name, grid_spec=grid_spec,
                  out_shape=[shp, shp, shp, jax.ShapeDtypeStruct((S, FOX_WIDTH), F32),
                             jax.ShapeDtypeStruct((N_FOX_HEADS, 1, S), F32)],
                  compiler_params=_params("parallel", "arbitrary"))(i_tab, j_tab, qp, kp, vp, dop)


def _dedilate(t, d):
    if d == 1:
        return t
    S, C = t.shape
    return t.reshape(S // d, d, C).transpose(1, 0, 2).reshape(S, C)


def _redilate(t, d):
    if d == 1:
        return t
    S, C = t.shape
    return t.reshape(d, S // d, C).transpose(1, 0, 2).reshape(S, C)


def _layer_step(x, tgt, w, p, late_weights=None, grad_sink=None, after=None, first_weights=None):
    S = x.shape[0]
    after_norm, after_proj = after if after is not None else (None, None)
    h = _rms_fwd(x, p["norm_mix_g"], name="rms_mix", after=after_norm)
    if first_weights is not None:
        w = {**w, **first_weights(h)}
    qkv = _mm(h, w["qkv"], name="proj_qkv", out_dtype=_CD, tn=768, tm=2048, after=after_proj)
    zf = _mm(h, w["f"], name="proj_f")
    gl = _mm(h, w["g"], name="proj_gate", tn=1024)

    dil_q, dil_k, dil_v = [], [], []
    dil_o, dil_l = [], []
    for g, (_, d) in enumerate(DIL_PAIRS):
        qg = _dedilate(qkv[:, g * DIL_OUT:(g + 1) * DIL_OUT], d)
        kg = _dedilate(qkv[:, DIL_WIDTH + g * DIL_OUT:DIL_WIDTH + (g + 1) * DIL_OUT], d)
        vg = _dedilate(qkv[:, 2 * DIL_WIDTH + g * DIL_OUT:2 * DIL_WIDTH + (g + 1) * DIL_OUT], d)
        og, lg = _dil_fwd(qg, kg, vg, g, name=f"dil_fwd{g}")
        dil_q.append(qg), dil_k.append(kg), dil_v.append(vg)
        dil_o.append(_redilate(og, d)), dil_l.append(_redilate(lg, d))
    o_a = _dil_mix_fwd(dil_o, dil_l, name="dil_mix")

    c = _fox_cumsum(zf, p["b_fgt"], name="fox_cumsum")
    fqp, fkp, fvp = _fox_pack_fwd(qkv, c, name="fox_pack")
    o_b, flse = _fox_fwd(fqp, fkp, fvp, name="fox_fwd")

    if late_weights is not None:
        w = {**w, **late_weights(o_b)}
    y_a = _mm(o_a, w["dil_out"], name="y_a", tn=1024, out_dtype=_CD)
    y_b = _mm(o_b, w["fox_out"], name="y_b", tn=1024, out_dtype=_CD)
    merged = _gate_fwd(gl, p["b_gate"], y_a, y_b, name="gate_fwd")
    x1 = _mm(merged, w["out"], name="mix_out", add=x)

    h2 = _rms_fwd(x1, p["norm_ffn_g"], name="rms_ffn")
    gu = _mm(h2, w["ffn_in"], name="ffn_in", tn=1408, b_blocks=True, out_dtype=_CD, tm=2048)
    act = _swiglu_fwd(gu, name="swiglu")
    x2 = _mm(act, w["ffn_down"], name="ffn_down", add=x1, tk=2816)

    loss, dx2, dg_final = _loss_head(x2, p["norm_final_g"], tgt, name="loss_head")

    dact = _mm(dx2, w["ffn_down"], name="d_act", tb=True, tn=1408, out_dtype=_CD)
    gw_ffn_down = _mm(act, dx2, name="gw_ffn_down", ta=True, out_dtype=_CD, tm=1408)
    dgu = _swiglu_bwd(dact, gu, name="swiglu_bwd")
    dh2 = _mm(dgu, w["ffn_in"], name="d_h2", tb=True, tk=1408, b_blocks=True, tm=2048)
    gw_ffn_in = _mm(h2, dgu, name="gw_ffn_in", ta=True, out_dtype=_CD, tn=1408, out_blocks=1408)
    sink = grad_sink if grad_sink is not None else (lambda group, grads: None)
    tok = sink("ffn", dict(ffn_in=gw_ffn_in, ffn_down=gw_ffn_down))
    dx1, dg_ffn = _rms_bwd(x1, p["norm_ffn_g"], dh2, dx2, name="rms_ffn_bwd", after=tok)

    dmerged = _mm(dx1, w["out"], name="d_merged", tb=True, out_dtype=_CD)
    gw_out = _mm(merged, dx1, name="gw_out", ta=True, out_dtype=_CD)
    dy_a, dy_b, dgl, db_gate = _gate_bwd(dmerged, gl, p["b_gate"], y_a, y_b, name="gate_bwd")
    do_a = _mm(dy_a, w["dil_out"], name="d_o_a", tb=True)
    gw_dil_out = _mm(o_a, dy_a, name="gw_dil_out", ta=True, out_dtype=_CD, tn=1024)
    do_b = _mm(dy_b, w["fox_out"], name="d_o_b", tb=True)
    gw_fox_out = _mm(o_b, dy_b, name="gw_fox_out", ta=True, out_dtype=_CD, tn=1024)
    tok = sink("mix", dict(dil_out=gw_dil_out, fox_out=gw_fox_out, out=gw_out))

    bqp, bdop = _fox_pack_bwd(qkv, c, o_b, flse, do_b, name="fox_pack_bwd", after=tok)
    dqp, dkp, dvp, dck, dcq = _fox_bwd(bqp, fkp, fvp, bdop, name="fox_bwd")
    dc = dcq[:, 0, :].T - dck.reshape(S, N_FOX_HEADS, HEAD_DIM)[:, :, 0]
    dc = jnp.pad(dc, ((0, 0), (0, F_PAD - N_FOX_HEADS)))
    dzf, db_fgt = _fox_cumsum_bwd(dc, zf, p["b_fgt"], name="fox_cumsum_bwd")

    douts = _dil_mix_bwd(do_a, dil_o, dil_l, name="dil_mix_bwd", after=tok)
    dqs, dks, dvs = [], [], []
    for g, (_, d) in enumerate(DIL_PAIRS):
        dq, dk, dv = _dil_bwd(dil_q[g], dil_k[g], dil_v[g], _dedilate(dil_o[g], d), _dedilate(dil_l[g], d),
                              _dedilate(douts[g], d), _dedilate(douts[3 + g], d), g, name=f"dil_bwd{g}")
        dqs.append(_redilate(dq, d)), dks.append(_redilate(dk, d)), dvs.append(_redilate(dv, d))
    dqkv = jnp.concatenate(dqs + dks + dvs + [dqp, dkp, dvp], axis=1)

    gw_qkv = _mm(h, dqkv, name="gw_qkv", ta=True, out_dtype=_CD, tn=768)
    gw_g = _mm(h, dgl, name="gw_gate", ta=True, out_dtype=_CD)
    gw_f = _mm(h, dzf, name="gw_f", ta=True, out_dtype=_CD)
    tok = sink("in", dict(qkv=gw_qkv, f=gw_f, g=gw_g))
    dh = _mm(dqkv, w["qkv"], name="d_h_qkv", tb=True, tk=1920, tm=2048, after=tok)
    dh = _mm(dgl, w["g"], name="d_h_gate", tb=True, add=dh)
    dh = _mm(dzf, w["f"], name="d_h_f", tb=True, add=dh)
    dx, dg_mix = _rms_bwd(x, p["norm_mix_g"], dh, dx1, name="rms_mix_bwd")

    gw = dict(qkv=gw_qkv, f=gw_f, g=gw_g, dil_out=gw_dil_out, fox_out=gw_fox_out, out=gw_out, ffn_in=gw_ffn_in,
              ffn_down=gw_ffn_down)
    small = dict(norm_mix_g=dg_mix, b_fgt=db_fgt, b_gate=db_gate, norm_ffn_g=dg_ffn, norm_final_g=dg_final)
    return loss, dx, gw, small


def _position():
    return lax.axis_index("x"), lax.axis_index("y"), lax.axis_index("c")


def _other_chips(x, y):
    return [(1 - x, y), (x, 1 - y), (1 - x, 1 - y)]


ROW_TILE = 16


def _row_chunks(rows, want=4):
    n = want
    while n > 1 and rows % (n * ROW_TILE):
        n //= 2
    return n


SEM_SPEC = pl.BlockSpec(memory_space=pltpu.SEMAPHORE)
ANY_SPEC = pl.BlockSpec(memory_space=pl.ANY)
DATAFLOW = pltpu.SideEffectType.DATAFLOW_SIDE_EFFECTING


def _in_hbm(a):
    return pltpu.with_memory_space_constraint(a, pltpu.HBM)


def _split_copy_start(srcs, land_shapes, copies, after, *, name):
    n, m = len(srcs), len(land_shapes)

    def body(*refs):
        src_refs, land_refs = refs[:n], refs[n:n + m]
        send_sems, recv_sems = refs[n + m + 1], refs[n + m + 2]
        token = refs[-1]
        x, y, c = _position()
        for k, (src, dst, peer) in enumerate(copies(x, y, c, src_refs, land_refs)):
            pltpu.make_async_remote_copy(src_ref=src, dst_ref=dst, send_sem=send_sems.at[k], recv_sem=recv_sems.at[k],
                                         device_id=peer, device_id_type=MESH).start()
        token[...] = jnp.zeros_like(token)

    lands = [lax.empty(s.shape, s.dtype) for s in land_shapes]
    count = len(copies(0, 0, 0, srcs, lands))
    out = _pcall(
        body, name=name,
        out_shape=(pltpu.SemaphoreType.DMA((count,)), pltpu.SemaphoreType.DMA((count,)),
                   *[pltpu.HBM(s.shape, s.dtype) for s in srcs], *[pltpu.HBM(s.shape, s.dtype) for s in land_shapes],
                   jax.ShapeDtypeStruct((8, 128), F32)),
        in_specs=[HBM_SPEC] * (n + m) + [ANY_SPEC],
        out_specs=(SEM_SPEC, SEM_SPEC, *[HBM_SPEC] * (n + m), pl.BlockSpec(memory_space=pltpu.VMEM)),
        input_output_aliases={k: 2 + k for k in range(n + m)},
        compiler_params=pltpu.CompilerParams(has_side_effects=DATAFLOW),
    )(*[_in_hbm(s) for s in srcs], *[_in_hbm(l) for l in lands], after)
    return out[0], out[1], list(out[2:2 + n]), list(out[2 + n:2 + n + m]), out[-1]


def _split_copy_wait(send_sems, recv_sems, srcs, lands, copies, after, *, name):
    n, m = len(srcs), len(lands)

    def body(*refs):
        src_refs, land_refs = refs[:n], refs[n:n + m]
        send, recv = refs[n + m], refs[n + m + 1]
        x, y, c = _position()
        for k, (src, dst, peer) in enumerate(copies(x, y, c, src_refs, land_refs)):
            cp = pltpu.make_async_remote_copy(src_ref=src, dst_ref=dst, send_sem=send.at[k], recv_sem=recv.at[k],
                                              device_id=peer, device_id_type=MESH)
            cp.wait_send()
            cp.wait_recv()

    afters = list(after) if isinstance(after, (list, tuple)) else [after]
    out = _pcall(
        body, name=name,
        out_shape=tuple(pltpu.HBM(s.shape, s.dtype) for s in list(srcs) + list(lands)),
        in_specs=[HBM_SPEC] * (n + m) + [SEM_SPEC, SEM_SPEC] + [ANY_SPEC] * len(afters),
        out_specs=tuple([HBM_SPEC] * (n + m)),
        input_output_aliases={k: k for k in range(n + m)},
        compiler_params=pltpu.CompilerParams(has_side_effects=DATAFLOW),
    )(*srcs, *lands, send_sems, recv_sems, *afters)
    return list(out[:n]), list(out[n:])


def _gather_copies(x, y, c, shard_refs, land_refs):
    out = []
    for s, l in zip(shard_refs, land_refs):
        half = s.shape[0] // 2
        nq = _row_chunks(half)
        for cx, cy in _other_chips(x, y):
            for q in range(nq):
                rows = pl.ds(c * half + q * (half // nq), half // nq)
                out.append((s.at[rows, :], l.at[2 * x + y, rows, :], (cx, cy, c)))
    return out


def _scatter_copies(x, y, c, part_refs, land_refs):
    out = []
    for p, l in zip(part_refs, land_refs):
        nq = _row_chunks(p.shape[1])
        for r, (cx, cy) in enumerate(_other_chips(x, y)):
            for q in range(nq):
                rows = pl.ds(q * (p.shape[1] // nq), p.shape[1] // nq)
                out.append((p.at[2 * cx + cy, rows, :], l.at[r, rows, :], (cx, cy, c)))
    return out


def _forward_halves(lands, *, name):
    n = len(lands)

    def body(*refs):
        ins = refs[:n]
        send_sems, recv_sems = refs[2 * n:]
        x, y, c = _position()
        copies = []
        for w in range(n):
            half = ins[w].shape[1] // 2
            for r, (cx, cy) in enumerate(_other_chips(x, y)):
                blk = ins[w].at[2 * cx + cy, pl.ds(c * half, half), :]
                cp = pltpu.make_async_remote_copy(src_ref=blk, dst_ref=blk, send_sem=send_sems.at[w, r],
                                                  recv_sem=recv_sems.at[w, r], device_id=(x, y, 1 - c),
                                                  device_id_type=MESH)
                cp.start()
                copies.append(cp)
        for w in range(n):
            half = ins[w].shape[1] // 2
            for r, (cx, cy) in enumerate(_other_chips(x, y)):
                blk = ins[w].at[2 * cx + cy, pl.ds((1 - c) * half, half), :]
                pltpu.make_async_remote_copy(src_ref=blk, dst_ref=blk, send_sem=send_sems.at[w, r],
                                             recv_sem=recv_sems.at[w, r], device_id=(x, y, 1 - c),
                                             device_id_type=MESH).wait_recv()
        for cp in copies:
            cp.wait_send()

    return _pcall(
        body, name=name, in_specs=[HBM_SPEC] * n, out_specs=[HBM_SPEC] * n,
        out_shape=[jax.ShapeDtypeStruct(l.shape, l.dtype) for l in lands],
        input_output_aliases={k: k for k in range(n)},
        scratch_shapes=[pltpu.SemaphoreType.DMA((n, 3)), pltpu.SemaphoreType.DMA((n, 3))],
    )(*lands)


def _swap_halves(grads, name="swap_halves"):
    n = len(grads)

    def body(*refs):
        ins, outs = refs[:n], refs[n:2 * n]
        send_sems, recv_sems = refs[2 * n:]
        x, y, c = _position()
        copies = []
        for w in range(n):
            half = ins[w].shape[1] // 2
            cp = pltpu.make_async_remote_copy(
                src_ref=ins[w].at[:, pl.ds((1 - c) * half, half), :], dst_ref=outs[w], send_sem=send_sems.at[w],
                recv_sem=recv_sems.at[w], device_id=(x, y, 1 - c), device_id_type=MESH)
            cp.start()
            copies.append(cp)
        for cp in copies:
            cp.wait()

    return _pcall(
        body, name=name, in_specs=[HBM_SPEC] * n, out_specs=[HBM_SPEC] * n,
        out_shape=[jax.ShapeDtypeStruct((4, g.shape[1] // 2, g.shape[2]), g.dtype) for g in grads],
        scratch_shapes=[pltpu.SemaphoreType.DMA((n,)), pltpu.SemaphoreType.DMA((n,))],
    )(*grads)


def _share_halves(halves):
    n = len(halves)

    def body(*refs):
        ins, outs = refs[:n], refs[n:2 * n]
        send_sems, recv_sems = refs[2 * n:]
        x, y, c = _position()
        copies = []
        for w in range(n):
            cp = pltpu.make_async_remote_copy(src_ref=ins[w], dst_ref=outs[w], send_sem=send_sems.at[w],
                                              recv_sem=recv_sems.at[w], device_id=(x, y, 1 - c), device_id_type=MESH)
            cp.start()
            copies.append(cp)
        for cp in copies:
            cp.wait()

    return _pcall(
        body, name="share_halves", in_specs=[HBM_SPEC] * n, out_specs=[HBM_SPEC] * n,
        out_shape=[jax.ShapeDtypeStruct(h.shape, h.dtype) for h in halves],
        scratch_shapes=[pltpu.SemaphoreType.DMA((n,)), pltpu.SemaphoreType.DMA((n,))],
    )(*halves)


def _sum_small(part):
    rows, width = part.shape

    def body(x_ref, out_ref, all_ref, send_sems, recv_sems):
        x, y, c = _position()
        me, sibling = (x, y, c), (x, y, 1 - c)
        chips = _other_chips(x, y)

        def block(px, py, pc):
            return all_ref.at[pl.ds((4 * px + 2 * py + pc) * rows, rows), :]

        def copy(k, blk, to, src=None):
            return pltpu.make_async_remote_copy(
                src_ref=block(*blk) if src is None else src, dst_ref=block(*blk), send_sem=send_sems.at[k],
                recv_sem=recv_sems.at[k], device_id=to, device_id_type=MESH)

        all_ref[pl.ds((4 * x + 2 * y + c) * rows, rows), :] = x_ref[...]
        first = [copy(0, me, sibling, src=x_ref)]
        first += [copy(1 + j, me, (*chip, c), src=x_ref) for j, chip in enumerate(chips)]
        for cp in first:
            cp.start()
        passed = [copy(4 + j, (*chip, c), sibling) for j, chip in enumerate(chips)]
        for j, chip in enumerate(chips):
            copy(1 + j, (*chip, c), me).wait_recv()
            passed[j].start()
        copy(0, sibling, me).wait_recv()
        for j, chip in enumerate(chips):
            copy(4 + j, (*chip, 1 - c), me).wait_recv()
        for cp in first + passed:
            cp.wait_send()
        total = all_ref[0:rows, :]
        for d in range(1, 8):
            total = total + all_ref[d * rows:(d + 1) * rows, :]
        out_ref[...] = total

    vm = pl.BlockSpec(memory_space=pltpu.VMEM)
    return _pcall(
        body, name="sum_small", in_specs=[vm], out_specs=vm, out_shape=jax.ShapeDtypeStruct((rows, width), F32),
        scratch_shapes=[pltpu.VMEM((8 * rows, width), F32), pltpu.SemaphoreType.DMA((7,)), pltpu.SemaphoreType.DMA((7,))],
    )(part)


def _row_tile(R, C, itemsize=4, budget=1 << 20):
    for t in (512, 256, 128, 64, 32, 16, 8):
        if R % t == 0 and t * C * itemsize <= budget:
            return t
    return R


def _add_halves(g, recv, c, *, name):
    _, R, C = g.shape
    half = R // 2
    t = _row_tile(half, C)
    nb = half // t

    def body(c_ref, g_ref, r_ref, o_ref):
        o_ref[...] = (g_ref[...].astype(F32) + r_ref[...].astype(F32)).astype(o_ref.dtype)

    grid_spec = pltpu.PrefetchScalarGridSpec(
        num_scalar_prefetch=1, grid=(4, nb),
        in_specs=[pl.BlockSpec((1, t, C), lambda k, i, cr: (k, cr[0] * nb + i, 0)),
                  pl.BlockSpec((1, t, C), lambda k, i, cr: (k, i, 0))],
        out_specs=pl.BlockSpec((1, t, C), lambda k, i, cr: (k, i, 0)))
    return _pcall(body, name=name, grid_spec=grid_spec, out_shape=jax.ShapeDtypeStruct((4, half, C), g.dtype),
                  compiler_params=_params("parallel", "parallel"))(c, g, recv)


def _add_owners(mine, recv, *, name):
    half, C = mine.shape
    t = _row_tile(half, C)

    def body(m_ref, r_ref, o_ref):
        o_ref[...] = ((m_ref[...].astype(F32) + r_ref[0].astype(F32)) + r_ref[1].astype(F32)) + r_ref[2].astype(F32)

    return _pcall(body, name=name, grid=(half // t,),
                  in_specs=[pl.BlockSpec((t, C), lambda i: (i, 0)), pl.BlockSpec((3, t, C), lambda i: (0, i, 0))],
                  out_specs=pl.BlockSpec((t, C), lambda i: (i, 0)), out_shape=jax.ShapeDtypeStruct((half, C), F32),
                  compiler_params=_params("parallel"))(mine, recv)


def _adamw(w, g, m, v, *, name):
    R, C = w.shape
    t = _row_tile(R, C)
    c1 = 1.0 - ADAM_B1 ** ADAM_STEP
    c2 = 1.0 - ADAM_B2 ** ADAM_STEP

    def body(w_ref, g_ref, m_ref, v_ref, d_ref, nm_ref, nv_ref):
        gv = g_ref[...]
        mn = ADAM_B1 * m_ref[...] + (1.0 - ADAM_B1) * gv
        vn = ADAM_B2 * v_ref[...] + (1.0 - ADAM_B2) * (gv * gv)
        d_ref[...] = -ADAM_LR * ((mn / c1) / (jnp.sqrt(vn / c2) + ADAM_EPS) + ADAM_WD * w_ref[...])
        nm_ref[...] = mn
        nv_ref[...] = vn

    blk = pl.BlockSpec((t, C), lambda i: (i, 0))
    shp = jax.ShapeDtypeStruct((R, C), F32)
    return _pcall(body, name=name, grid=(R // t,), in_specs=[blk] * 4, out_specs=[blk] * 3, out_shape=[shp] * 3,
                  compiler_params=_params("parallel"))(w, g, m, v)


BIG = ("w_in", "w_dil_out", "w_fox_out", "w_out", "w_ffn_in", "w_ffn_down")
SMALL = ("norm_mix_g", "b_fgt", "b_gate", "norm_ffn_g", "norm_final_g")
ORDER = ("norm_mix_g", "w_in", "b_fgt", "b_gate", "w_dil_out", "w_fox_out", "w_out", "norm_ffn_g", "w_ffn_in",
         "w_ffn_down", "norm_final_g")
SMALL_ROWS = {"norm_mix_g": (0, 1), "b_gate": (1, 3), "norm_ffn_g": (3, 4), "norm_final_g": (4, 5), "b_fgt": (5, 6)}


def _columns_to_blocks(full, ncol):
    K = full.shape[0]
    return full.reshape(K, 4, ncol).transpose(1, 0, 2)


def _blocks_to_columns(blocks):
    n, K, ncol = blocks.shape
    return blocks.transpose(1, 0, 2).reshape(K, n * ncol)


def kernel(x, norm_mix_g, w_in, b_fgt, b_gate, w_dil_out, w_fox_out, w_out, norm_ffn_g, w_ffn_in, w_ffn_down, norm_final_g, loss_target, m_norm_mix_g, m_w_in, m_b_fgt, m_b_gate, m_w_dil_out, m_w_fox_out, m_w_out, m_norm_ffn_g, m_w_ffn_in, m_w_ffn_down, m_norm_final_g, v_norm_mix_g, v_w_in, v_b_fgt, v_b_gate, v_w_dil_out, v_w_fox_out, v_w_out, v_norm_ffn_g, v_w_ffn_in, v_w_ffn_down, v_norm_final_g):
    weights = dict(norm_mix_g=norm_mix_g, w_in=w_in, b_fgt=b_fgt, b_gate=b_gate, w_dil_out=w_dil_out,
                   w_fox_out=w_fox_out, w_out=w_out, norm_ffn_g=norm_ffn_g, w_ffn_in=w_ffn_in, w_ffn_down=w_ffn_down,
                   norm_final_g=norm_final_g)
    m_in = dict(norm_mix_g=m_norm_mix_g, w_in=m_w_in, b_fgt=m_b_fgt, b_gate=m_b_gate, w_dil_out=m_w_dil_out,
                w_fox_out=m_w_fox_out, w_out=m_w_out, norm_ffn_g=m_norm_ffn_g, w_ffn_in=m_w_ffn_in,
                w_ffn_down=m_w_ffn_down, norm_final_g=m_norm_final_g)
    v_in = dict(norm_mix_g=v_norm_mix_g, w_in=v_w_in, b_fgt=v_b_fgt, b_gate=v_b_gate, w_dil_out=v_w_dil_out,
                w_fox_out=v_w_fox_out, w_out=v_w_out, norm_ffn_g=v_norm_ffn_g, w_ffn_in=v_w_ffn_in,
                w_ffn_down=v_w_ffn_down, norm_final_g=v_norm_final_g)
    c = lax.axis_index("c")
    chip = 2 * lax.axis_index("x") + lax.axis_index("y")

    shards = {n: weights[n][0].astype(_CD) for n in BIG}
    in_shape = jax.ShapeDtypeStruct((4,) + shards["w_in"].shape, _CD)
    send_i, recv_i, in_src, in_land, token_in = _split_copy_start(
        [shards["w_in"]], [in_shape], _gather_copies, norm_mix_g, name="gather_in_start")
    late = BIG[1:]
    send_g, recv_g, late_src, late_land, token = _split_copy_start(
        [shards[n] for n in late], [jax.ShapeDtypeStruct((4,) + shards[n].shape, _CD) for n in late],
        _gather_copies, token_in, name="gather_late_start")
    adam_in = [t[0] + token_in[0, 0] for t in (w_in, m_w_in, v_w_in)]
    p = dict(norm_mix_g=norm_mix_g, b_fgt=jnp.pad(b_fgt, ((0, 0), (0, F_PAD - N_FOX_HEADS))), b_gate=b_gate,
             norm_ffn_g=norm_ffn_g, norm_final_g=norm_final_g.reshape(1, D_MODEL))

    def first_weights(after):
        own, lands = _split_copy_wait(send_i, recv_i, in_src, in_land, _gather_copies, [after] + adam_in,
                                      name="gather_in_wait")
        (g_in,) = _forward_halves(lands, name="gather_in_forward")
        full_in = _blocks_to_columns(lax.dynamic_update_index_in_dim(g_in, own[0], chip, 0))
        o3 = QKV_COLS
        o4 = o3 + N_FOX_HEADS
        return dict(qkv=full_in[:, :o3], f=jnp.pad(full_in[:, o3:o4], ((0, 0), (0, F_PAD - N_FOX_HEADS))),
                    g=full_in[:, o4:])

    def late_weights(after):
        own, lands = _split_copy_wait(send_g, recv_g, late_src, late_land, _gather_copies, after,
                                      name="gather_late_wait")
        lands = _forward_halves(lands, name="gather_late_forward")
        g_dil, g_fox, g_out, g_ffn_in, g_ffn_down = [
            lax.dynamic_update_index_in_dim(l, s, chip, 0) for l, s in zip(lands, own)]
        return dict(dil_out=_blocks_to_columns(g_dil), fox_out=_blocks_to_columns(g_fox),
                    out=g_out.reshape(D_MODEL, D_MODEL), ffn_in=g_ffn_in,
                    ffn_down=g_ffn_down.reshape(D_FF, D_MODEL))

    c_arr = jnp.reshape(c, (1,)).astype(jnp.int32)

    def to_blocks(n, full):
        shape = weights[n].shape
        if full.ndim == 3:
            return full
        if n in ("w_out", "w_ffn_down"):
            return full.reshape(4, shape[1], shape[2])
        return _columns_to_blocks(full, shape[2])

    def pair_sums(group, named):
        names = list(named)
        blocks = [to_blocks(n, named[n]) for n in names]
        from_sibling = _swap_halves(blocks, name=f"swap_halves_{group}")
        return [_add_halves(b, r, c_arr, name=f"add_halves_{n}") for b, r, n in zip(blocks, from_sibling, names)]

    in_flight = {}

    def grad_sink(group, gw):
        if group == "in":
            named = {"w_in": jnp.concatenate([gw["qkv"], gw["f"][:, :N_FOX_HEADS], gw["g"]], axis=1)}
        else:
            named = {"w_" + k: v for k, v in gw.items()}
        sums = pair_sums(group, named)
        started = _split_copy_start(sums, [jax.ShapeDtypeStruct((3,) + s.shape[1:], s.dtype) for s in sums],
                                    _scatter_copies, next(iter(gw.values())), name=f"scatter_{group}_start")
        in_flight[group] = (list(named), started)
        return started[-1]

    loss_part, grad_x, gw, small = _layer_step(x[0], loss_target[0], {}, p, late_weights, grad_sink,
                                               (token_in, token), first_weights)

    def owner_sums(names, sums, from_chips):
        return {n: _add_owners(lax.dynamic_index_in_dim(s, chip, 0, keepdims=False), r, name=f"add_owners_{n}")
                for n, s, r in zip(names, sums, from_chips)}

    halves = {}
    for group, (names, (send_s, recv_s, srcs, lands, _)) in in_flight.items():
        sums, from_chips = _split_copy_wait(send_s, recv_s, srcs, lands, _scatter_copies, grad_x,
                                            name=f"scatter_{group}_wait")
        halves.update(owner_sums(names, sums, from_chips))
    halves = [halves[n] for n in BIG]
    grads = {}
    for n, own, other in zip(BIG, halves, _share_halves(halves)):
        pair = jnp.stack([own, other])
        grads[n] = jnp.where(c == 0, pair, pair[::-1]).reshape(2 * own.shape[0], own.shape[1])

    packed = jnp.concatenate([
        small["norm_mix_g"], small["b_gate"].reshape(2, D_MODEL), small["norm_ffn_g"], small["norm_final_g"],
        jnp.pad(small["b_fgt"], ((0, 0), (0, D_MODEL - F_PAD))), jnp.zeros((2, D_MODEL), F32)], axis=0)
    summed = _sum_small(packed)
    for n in SMALL:
        lo, hi = SMALL_ROWS[n]
        grads[n] = summed[lo:hi].reshape(1, -1)[:, :weights[n].size]

    loss = lax.psum(loss_part[0, 0], ("x", "y", "c"))

    out_g, out_d, out_m, out_v = {}, {}, {}, {}
    for n in ORDER:
        shape = weights[n].shape
        two_d = shape[1:] if len(shape) == 3 else (1, weights[n].size)
        g2 = grads[n].reshape(two_d)
        wmv = adam_in if n == "w_in" else [t.reshape(two_d) for t in (weights[n], m_in[n], v_in[n])]
        d2, m2, v2 = _adamw(wmv[0], g2, wmv[1], wmv[2], name=f"adamw_{n}")
        out_g[n], out_d[n], out_m[n], out_v[n] = (g2.reshape(shape), d2.reshape(shape), m2.reshape(shape),
                                                  v2.reshape(shape))
    return (loss, grad_x[None], *[out_g[n] for n in ORDER], *[out_d[n] for n in ORDER],
            *[out_m[n] for n in ORDER], *[out_v[n] for n in ORDER])
```

```python
import numpy as np
import jax
import jax.numpy as jnp
from jax import lax
from jax.experimental import pallas as pl
from jax.experimental.pallas import tpu as pltpu

F32 = jnp.float32
_CD = jnp.bfloat16

D_MODEL = 1024
HEAD_DIM = 64
DIL_PAIRS = ((128, 1), (512, 4), (2048, 16))
N_DIL_GROUPS = 3
DIL_HEADS = 4
DIL_W = 128
DIL_OUT = DIL_HEADS * HEAD_DIM
DIL_WIDTH = N_DIL_GROUPS * DIL_OUT
N_FOX_HEADS = 8
FOX_WIDTH = N_FOX_HEADS * HEAD_DIM
D_FF = 2816
QKV_COLS = 3 * DIL_WIDTH + 3 * FOX_WIDTH
F_PAD = 128
RMS_EPS = 1e-6
NEG_INF = -1e30
ATTN_SCALE = HEAD_DIM ** -0.5
ADAM_LR, ADAM_B1, ADAM_B2, ADAM_EPS, ADAM_WD, ADAM_STEP = 0.001, 0.9, 0.999, 1e-08, 0.01, 10

VMEM_LIMIT = 48 * 1024 * 1024
MESH = pl.DeviceIdType.MESH
HBM_SPEC = pl.BlockSpec(memory_space=pltpu.HBM)


def _pcall(body, after=None, **kw):
    if after is None:
        return pl.pallas_call(body, **kw)
    n_in = len(kw["in_specs"])
    kw["in_specs"] = list(kw["in_specs"]) + [pl.BlockSpec(memory_space=pl.ANY)]

    def tied(*refs):
        return body(*refs[:n_in], *refs[n_in + 1:])

    call = pl.pallas_call(tied, **kw)
    return lambda *args: call(*args, after)


def _params(*sem):
    return pltpu.CompilerParams(dimension_semantics=sem, vmem_limit_bytes=VMEM_LIMIT)


def _pick(dim, pref):
    t = (min(pref, dim) // 128) * 128
    while t >= 128:
        if dim % t == 0:
            return t
        t -= 128
    return dim


def _mm(a, b, *, name, ta=False, tb=False, out_dtype=F32, add=None, tm=1024, tn=512, tk=2048, after=None,
        b_blocks=False, out_blocks=None):
    if ta:
        K, M = a.shape
    else:
        M, K = a.shape
    b_rows, b_cols = (b.shape[1], b.shape[0] * b.shape[2]) if b_blocks else b.shape
    if tb:
        N, K2 = b_rows, b_cols
    else:
        K2, N = b_rows, b_cols
    assert K == K2, (a.shape, b.shape)
    shard = b.shape[2] if b_blocks else None
    tm = _pick(M, tm)
    tn = _pick(shard if (b_blocks and not tb) else (out_blocks or N), tn)
    tk = _pick(shard if (b_blocks and tb) else K, tk)
    nk = K // tk
    dn = (((0 if ta else 1,), (1 if tb else 0,)), ((), ()))
    has_add = add is not None
    assert not (has_add and out_blocks)

    def body(*refs):
        a_ref, b_ref = refs[0], refs[1]
        add_ref = refs[2] if has_add else None
        o_ref = refs[3] if has_add else refs[2]
        bv = b_ref[0] if b_blocks else b_ref[...]
        p = lax.dot_general(a_ref[...].astype(_CD), bv.astype(_CD), dn, preferred_element_type=F32)

        def finish(r):
            if has_add:
                r = r + add_ref[...]
            if out_blocks:
                o_ref[0] = r.astype(out_dtype)
            else:
                o_ref[...] = r.astype(out_dtype)

        if nk == 1:
            finish(p)
        else:
            acc_ref = refs[-1]
            k = pl.program_id(2)

            @pl.when(k == 0)
            def _():
                acc_ref[...] = p

            @pl.when(k > 0)
            def _():
                acc_ref[...] += p

            @pl.when(k == nk - 1)
            def _():
                finish(acc_ref[...])

    a_spec = pl.BlockSpec((tk, tm), lambda i, j, k: (k, i)) if ta else pl.BlockSpec((tm, tk), lambda i, j, k: (i, k))
    if b_blocks and tb:
        per = shard // tk
        b_spec = pl.BlockSpec((1, tn, tk), lambda i, j, k: (k // per, j, k % per))
    elif b_blocks:
        per = shard // tn
        b_spec = pl.BlockSpec((1, tk, tn), lambda i, j, k: (j // per, k, j % per))
    else:
        b_spec = pl.BlockSpec((tn, tk), lambda i, j, k: (j, k)) if tb else pl.BlockSpec((tk, tn), lambda i, j, k: (k, j))
    if out_blocks:
        oper = out_blocks // tn
        o_spec = pl.BlockSpec((1, tm, tn), lambda i, j, k: (j // oper, i, j % oper))
        out_shape = jax.ShapeDtypeStruct((N // out_blocks, M, out_blocks), out_dtype)
    else:
        o_spec = pl.BlockSpec((tm, tn), lambda i, j, k: (i, j))
        out_shape = jax.ShapeDtypeStruct((M, N), out_dtype)
    in_specs = [a_spec, b_spec] + ([o_spec] if has_add else [])
    args = (a, b) + ((add,) if has_add else ())
    return _pcall(
        body, after, name=name, grid=(M // tm, N // tn, nk), in_specs=in_specs, out_specs=o_spec,
        out_shape=out_shape,
        scratch_shapes=[pltpu.VMEM((tm, tn), F32)] if nk > 1 else [],
        compiler_params=_params("parallel", "parallel", "arbitrary"),
    )(*args)


def _rms_fwd(x, g, *, name, tm=512, after=None):
    S, D = x.shape

    def body(x_ref, g_ref, h_ref):
        xv = x_ref[...]
        r = lax.rsqrt(jnp.mean(xv * xv, axis=-1, keepdims=True) + RMS_EPS)
        h_ref[...] = ((xv * r) * g_ref[...]).astype(h_ref.dtype)

    row = pl.BlockSpec((tm, D), lambda i: (i, 0))
    return _pcall(body, after, name=name, grid=(S // tm,), in_specs=[row, pl.BlockSpec((1, D), lambda i: (0, 0))],
                  out_specs=row, out_shape=jax.ShapeDtypeStruct((S, D), _CD), compiler_params=_params("parallel"))(x, g)


def _rms_bwd(x, g, dh, dres, *, name, tm=512, after=None):
    S, D = x.shape

    def body(x_ref, g_ref, dh_ref, dres_ref, dx_ref, dg_ref):
        xv = x_ref[...]
        r = lax.rsqrt(jnp.mean(xv * xv, axis=-1, keepdims=True) + RMS_EPS)
        xh = xv * r
        dhv = dh_ref[...]
        dxh = dhv * g_ref[...]
        dx_ref[...] = dres_ref[...] + r * (dxh - xh * jnp.mean(dxh * xh, axis=-1, keepdims=True))
        part = jnp.sum(dhv * xh, axis=0, keepdims=True)

        @pl.when(pl.program_id(0) == 0)
        def _():
            dg_ref[...] = part

        @pl.when(pl.program_id(0) > 0)
        def _():
            dg_ref[...] += part

    row = pl.BlockSpec((tm, D), lambda i: (i, 0))
    vec = pl.BlockSpec((1, D), lambda i: (0, 0))
    return _pcall(body, after, name=name, grid=(S // tm,), in_specs=[row, vec, row, row], out_specs=[row, vec],
                  out_shape=[jax.ShapeDtypeStruct((S, D), F32), jax.ShapeDtypeStruct((1, D), F32)],
                  compiler_params=_params("arbitrary"))(x, g, dh, dres)


def _loss_head(x, g, tgt, *, name, tm=512):
    S, D = x.shape

    def body(x_ref, g_ref, t_ref, loss_ref, dx_ref, dg_ref):
        xv = x_ref[...]
        gv = g_ref[...]
        r = lax.rsqrt(jnp.mean(xv * xv, axis=-1, keepdims=True) + RMS_EPS)
        xh = xv * r
        err = xh * gv - t_ref[...]
        lpart = 0.5 * jnp.sum(jnp.mean(err * err, axis=-1, keepdims=True), axis=0, keepdims=True)
        dy = err * (1.0 / D)
        dxh = dy * gv
        dx_ref[...] = r * (dxh - xh * jnp.mean(dxh * xh, axis=-1, keepdims=True))
        gpart = jnp.sum(dy * xh, axis=0, keepdims=True)

        @pl.when(pl.program_id(0) == 0)
        def _():
            loss_ref[...] = lpart
            dg_ref[...] = gpart

        @pl.when(pl.program_id(0) > 0)
        def _():
            loss_ref[...] += lpart
            dg_ref[...] += gpart

    row = pl.BlockSpec((tm, D), lambda i: (i, 0))
    vec = pl.BlockSpec((1, D), lambda i: (0, 0))
    one = pl.BlockSpec((1, 1), lambda i: (0, 0))
    return _pcall(body, name=name, grid=(S // tm,), in_specs=[row, vec, row], out_specs=[one, row, vec],
                  out_shape=[jax.ShapeDtypeStruct((1, 1), F32), jax.ShapeDtypeStruct((S, D), F32),
                             jax.ShapeDtypeStruct((1, D), F32)],
                  compiler_params=_params("arbitrary"))(x, g, tgt)


def _sigmoid(z):
    return 1.0 / (1.0 + jnp.exp(-z))


def _gate_fwd(gl, bg, ya, yb, *, name, tm=512):
    S, D = ya.shape

    def body(za_ref, zb_ref, ba_ref, bb_ref, ya_ref, yb_ref, o_ref):
        ga = _sigmoid(za_ref[...].astype(F32) + ba_ref[...])
        gb = _sigmoid(zb_ref[...].astype(F32) + bb_ref[...])
        o_ref[...] = (ga * ya_ref[...].astype(F32) + gb * yb_ref[...].astype(F32)).astype(o_ref.dtype)

    lo = pl.BlockSpec((tm, D), lambda i: (i, 0))
    hi = pl.BlockSpec((tm, D), lambda i: (i, 1))
    vlo = pl.BlockSpec((1, D), lambda i: (0, 0))
    vhi = pl.BlockSpec((1, D), lambda i: (0, 1))
    return _pcall(body, name=name, grid=(S // tm,), in_specs=[lo, hi, vlo, vhi, lo, lo], out_specs=lo,
                  out_shape=jax.ShapeDtypeStruct((S, D), _CD), compiler_params=_params("parallel"))(gl, gl, bg, bg, ya, yb)


def _gate_bwd(dm, gl, bg, ya, yb, *, name, tm=256):
    S, D = ya.shape

    def body(dm_ref, za_ref, zb_ref, ba_ref, bb_ref, ya_ref, yb_ref, dya_ref, dyb_ref, dgl_ref, dbg_ref):
        dmv = dm_ref[...].astype(F32)
        ga = _sigmoid(za_ref[...].astype(F32) + ba_ref[...])
        gb = _sigmoid(zb_ref[...].astype(F32) + bb_ref[...])
        dya_ref[...] = (dmv * ga).astype(dya_ref.dtype)
        dyb_ref[...] = (dmv * gb).astype(dyb_ref.dtype)
        dza = dmv * ya_ref[...].astype(F32) * ga * (1.0 - ga)
        dzb = dmv * yb_ref[...].astype(F32) * gb * (1.0 - gb)
        dgl_ref[:, :D] = dza.astype(dgl_ref.dtype)
        dgl_ref[:, D:] = dzb.astype(dgl_ref.dtype)
        pa = jnp.sum(dza, axis=0, keepdims=True)
        pb = jnp.sum(dzb, axis=0, keepdims=True)

        @pl.when(pl.program_id(0) == 0)
        def _():
            dbg_ref[:, :D] = pa
            dbg_ref[:, D:] = pb

        @pl.when(pl.program_id(0) > 0)
        def _():
            dbg_ref[:, :D] += pa
            dbg_ref[:, D:] += pb

    lo = pl.BlockSpec((tm, D), lambda i: (i, 0))
    hi = pl.BlockSpec((tm, D), lambda i: (i, 1))
    vlo = pl.BlockSpec((1, D), lambda i: (0, 0))
    vhi = pl.BlockSpec((1, D), lambda i: (0, 1))
    wide = pl.BlockSpec((tm, 2 * D), lambda i: (i, 0))
    vwide = pl.BlockSpec((1, 2 * D), lambda i: (0, 0))
    return _pcall(body, name=name, grid=(S // tm,), in_specs=[lo, lo, hi, vlo, vhi, lo, lo],
                  out_specs=[lo, lo, wide, vwide],
                  out_shape=[jax.ShapeDtypeStruct((S, D), _CD), jax.ShapeDtypeStruct((S, D), _CD),
                             jax.ShapeDtypeStruct((S, 2 * D), _CD), jax.ShapeDtypeStruct((1, 2 * D), F32)],
                  compiler_params=_params("arbitrary"))(dm, gl, gl, bg, bg, ya, yb)


def _swiglu_fwd(gu, *, name, tm=256):
    S, F2 = gu.shape
    F = F2 // 2

    def body(g_ref, u_ref, o_ref):
        gv = g_ref[...].astype(F32)
        o_ref[...] = (gv * _sigmoid(gv) * u_ref[...].astype(F32)).astype(o_ref.dtype)

    lo = pl.BlockSpec((tm, F), lambda i: (i, 0))
    hi = pl.BlockSpec((tm, F), lambda i: (i, 1))
    return _pcall(body, name=name, grid=(S // tm,), in_specs=[lo, hi], out_specs=lo,
                  out_shape=jax.ShapeDtypeStruct((S, F), _CD), compiler_params=_params("parallel"))(gu, gu)


def _swiglu_bwd(dact, gu, *, name, tm=256):
    S, F2 = gu.shape
    F = F2 // 2

    def body(d_ref, g_ref, u_ref, o_ref):
        dv = d_ref[...].astype(F32)
        gv = g_ref[...].astype(F32)
        sg = _sigmoid(gv)
        o_ref[:, :F] = (dv * u_ref[...].astype(F32) * (sg * (1.0 + gv * (1.0 - sg)))).astype(o_ref.dtype)
        o_ref[:, F:] = (dv * (gv * sg)).astype(o_ref.dtype)

    lo = pl.BlockSpec((tm, F), lambda i: (i, 0))
    hi = pl.BlockSpec((tm, F), lambda i: (i, 1))
    return _pcall(body, name=name, grid=(S // tm,), in_specs=[lo, lo, hi],
                  out_specs=pl.BlockSpec((tm, F2), lambda i: (i, 0)),
                  out_shape=jax.ShapeDtypeStruct((S, F2), _CD), compiler_params=_params("parallel"))(dact, gu, gu)


def _split3(x):
    hi = x.astype(jnp.bfloat16)
    r1 = x - hi.astype(F32)
    mid = r1.astype(jnp.bfloat16)
    lo = (r1 - mid.astype(F32)).astype(jnp.bfloat16)
    return hi, mid, lo


def _ones_dot_left(ones, x):
    return sum(jnp.dot(ones, p, preferred_element_type=F32) for p in _split3(x))


def _ones_dot_right(x, ones):
    return sum(jnp.dot(p, ones, preferred_element_type=F32) for p in _split3(x))


def _head_sum(x):
    n = x.shape[1]
    r = lax.broadcasted_iota(jnp.int32, (n, n), 0) // HEAD_DIM
    c = lax.broadcasted_iota(jnp.int32, (n, n), 1) // HEAD_DIM
    return _ones_dot_right(x, (r == c).astype(jnp.bfloat16))


def _log_sigmoid(z):
    e = jnp.exp(-jnp.abs(z))
    t = 1.0 + e
    log1p_e = jnp.where(t == 1.0, e, jnp.log(t) * (e / jnp.where(t == 1.0, 1.0, t - 1.0)))
    return jnp.minimum(z, 0.0) - log1p_e


def _fox_cumsum(zf, bf, *, name):
    S, W = zf.shape
    nb = S // 128

    def body(z_ref, b_ref, c_ref):
        tri = (lax.broadcasted_iota(jnp.int32, (128, 128), 0) >= lax.broadcasted_iota(jnp.int32, (128, 128), 1))
        tri = tri.astype(jnp.bfloat16)

        def step(i, carry):
            rows = pl.ds(pl.multiple_of(i * 128, 128), 128)
            lf = _log_sigmoid(z_ref[rows, :] + b_ref[...])
            cb = _ones_dot_left(tri, lf) + carry
            c_ref[rows, :] = cb
            return cb[127:128, :]

        lax.fori_loop(0, nb, step, jnp.zeros((1, W), F32))

    return _pcall(body, name=name, out_shape=jax.ShapeDtypeStruct((S, W), F32),
                  compiler_params=pltpu.CompilerParams(vmem_limit_bytes=VMEM_LIMIT))(zf, bf)


def _fox_cumsum_bwd(dc, zf, bf, *, name):
    S, W = zf.shape
    nb = S // 128

    def body(dc_ref, z_ref, b_ref, dz_ref, db_ref):
        tri = (lax.broadcasted_iota(jnp.int32, (128, 128), 0) <= lax.broadcasted_iota(jnp.int32, (128, 128), 1))
        tri = tri.astype(jnp.bfloat16)

        def step(k, carry):
            tail, acc = carry
            i = nb - 1 - k
            rows = pl.ds(pl.multiple_of(i * 128, 128), 128)
            dlf = _ones_dot_left(tri, dc_ref[rows, :]) + tail
            dz = dlf * _sigmoid(-(z_ref[rows, :] + b_ref[...]))
            dz_ref[rows, :] = dz
            return dlf[0:1, :], acc + jnp.sum(dz, axis=0, keepdims=True)

        _, acc = lax.fori_loop(0, nb, step, (jnp.zeros((1, W), F32), jnp.zeros((1, W), F32)))
        db_ref[...] = acc

    return _pcall(body, name=name,
                  out_shape=[jax.ShapeDtypeStruct((S, W), F32), jax.ShapeDtypeStruct((1, W), F32)],
                  compiler_params=pltpu.CompilerParams(vmem_limit_bytes=VMEM_LIMIT))(dc, zf, bf)


def _dil_slopes(group):
    h = np.arange(1, N_DIL_GROUPS * DIL_HEADS + 1, dtype=np.float32)
    s = (np.float32(2.0) ** (np.float32(-8.0) * h / np.float32(N_DIL_GROUPS * DIL_HEADS))).astype(np.float32)
    return [float(v) for v in s.reshape(N_DIL_GROUPS, DIL_HEADS)[group]]


def _dil_tiles(i, n, blocks_per_seq):
    qi = lax.broadcasted_iota(jnp.int32, (DIL_W, DIL_W), 0)
    kj = lax.broadcasted_iota(jnp.int32, (DIL_W, DIL_W), 1)
    first = ((4 * n + i) % blocks_per_seq) == 0
    valid_prev = jnp.logical_and(kj >= qi, jnp.logical_not(first))
    valid_cur = kj <= qi
    rel_prev = (qi - kj + DIL_W).astype(F32)
    rel_cur = (qi - kj).astype(F32)
    return valid_prev, valid_cur, rel_prev, rel_cur


CHUNK = 4 * DIL_W


def _dil_fwd(q, k, v, group, *, name):
    S = q.shape[0]
    dilation = DIL_PAIRS[group][1]
    bps = (S // dilation) // DIL_W
    slopes = _dil_slopes(group)
    nt = (((1,), (1,)), ((), ()))

    def body(q_ref, k_ref, v_ref, kp_ref, vp_ref, o_ref, l_ref):
        n = pl.program_id(0)
        for i in range(4):
            valid_prev, valid_cur, rel_prev, rel_cur = _dil_tiles(i, n, bps)
            rows = slice(i * DIL_W, (i + 1) * DIL_W)
            prow = slice((i - 1) * DIL_W, i * DIL_W)
            for h in range(DIL_HEADS):
                cols = slice(h * HEAD_DIM, (h + 1) * HEAD_DIM)
                qh = q_ref[rows, cols]
                kc, vc = k_ref[rows, cols], v_ref[rows, cols]
                kp = kp_ref[:, cols] if i == 0 else k_ref[prow, cols]
                vp = vp_ref[:, cols] if i == 0 else v_ref[prow, cols]
                sl = slopes[h] * dilation
                sp = lax.dot_general(qh, kp, nt, preferred_element_type=F32) * ATTN_SCALE - sl * rel_prev
                sc = lax.dot_general(qh, kc, nt, preferred_element_type=F32) * ATTN_SCALE - sl * rel_cur
                sp = jnp.where(valid_prev, sp, NEG_INF)
                sc = jnp.where(valid_cur, sc, NEG_INF)
                m = jnp.maximum(jnp.max(sp, axis=-1, keepdims=True), jnp.max(sc, axis=-1, keepdims=True))
                pp, pc = jnp.exp(sp - m), jnp.exp(sc - m)
                den = jnp.sum(pp, axis=-1, keepdims=True) + jnp.sum(pc, axis=-1, keepdims=True)
                acc = (jnp.dot(pp.astype(_CD), vp, preferred_element_type=F32)
                       + jnp.dot(pc.astype(_CD), vc, preferred_element_type=F32))
                o_ref[rows, cols] = acc / den
                l_ref[rows, cols] = jnp.broadcast_to(m + jnp.log(den), (DIL_W, HEAD_DIM))

    cur = pl.BlockSpec((CHUNK, DIL_OUT), lambda n: (n, 0))
    prev = pl.BlockSpec((DIL_W, DIL_OUT), lambda n: (jnp.maximum(4 * n - 1, 0), 0))
    return _pcall(body, name=name, grid=(S // CHUNK,), in_specs=[cur, cur, cur, prev, prev], out_specs=[cur, cur],
                  out_shape=[jax.ShapeDtypeStruct((S, DIL_OUT), F32), jax.ShapeDtypeStruct((S, DIL_OUT), F32)],
                  compiler_params=_params("parallel"))(q, k, v, k, v)


def _dil_bwd(q, k, v, o, lse, do, dlse, group, *, name):
    S = q.shape[0]
    dilation = DIL_PAIRS[group][1]
    bps = (S // dilation) // DIL_W
    slopes = _dil_slopes(group)
    nchunk = S // CHUNK
    nt = (((1,), (1,)), ((), ()))
    tn = (((0,), (0,)), ((), ()))

    def body(q_ref, k_ref, v_ref, kp_ref, vp_ref, o_ref, l_ref, do_ref, dl_ref, dq_ref, dk_ref, dv_ref,
             dk_s, dv_s):
        step = pl.program_id(0)
        n = nchunk - 1 - step

        @pl.when(step == 0)
        def _():
            dk_s[CHUNK:, :] = jnp.zeros((DIL_W, DIL_OUT), F32)
            dv_s[CHUNK:, :] = jnp.zeros((DIL_W, DIL_OUT), F32)

        dk_s[:CHUNK, :] = jnp.zeros((CHUNK, DIL_OUT), F32)
        dv_s[:CHUNK, :] = jnp.zeros((CHUNK, DIL_OUT), F32)
        for i in range(4):
            valid_prev, valid_cur, rel_prev, rel_cur = _dil_tiles(i, n, bps)
            rows = slice(i * DIL_W, (i + 1) * DIL_W)
            prow = slice((i - 1) * DIL_W, i * DIL_W)
            s_prev = slice(i * DIL_W, (i + 1) * DIL_W)
            s_cur = slice((i + 1) * DIL_W, (i + 2) * DIL_W)
            for h in range(DIL_HEADS):
                cols = slice(h * HEAD_DIM, (h + 1) * HEAD_DIM)
                qh = q_ref[rows, cols]
                kc, vc = k_ref[rows, cols], v_ref[rows, cols]
                kp = kp_ref[:, cols] if i == 0 else k_ref[prow, cols]
                vp = vp_ref[:, cols] if i == 0 else v_ref[prow, cols]
                sl = slopes[h] * dilation
                lh = l_ref[rows, h * HEAD_DIM:h * HEAD_DIM + 1]
                sp = lax.dot_general(qh, kp, nt, preferred_element_type=F32) * ATTN_SCALE - sl * rel_prev
                sc = lax.dot_general(qh, kc, nt, preferred_element_type=F32) * ATTN_SCALE - sl * rel_cur
                pp = jnp.exp(jnp.where(valid_prev, sp, NEG_INF) - lh)
                pc = jnp.exp(jnp.where(valid_cur, sc, NEG_INF) - lh)
                doh = do_ref[rows, cols]
                dsum = jnp.sum(doh * o_ref[rows, cols], axis=-1, keepdims=True)
                shift = dl_ref[rows, h * HEAD_DIM:h * HEAD_DIM + 1] - dsum
                dob = doh.astype(_CD)
                dsp = pp * (lax.dot_general(dob, vp, nt, preferred_element_type=F32) + shift)
                dsc = pc * (lax.dot_general(dob, vc, nt, preferred_element_type=F32) + shift)
                dspb = (dsp * ATTN_SCALE).astype(_CD)
                dscb = (dsc * ATTN_SCALE).astype(_CD)
                dq_ref[rows, cols] = (jnp.dot(dspb, kp, preferred_element_type=F32)
                                      + jnp.dot(dscb, kc, preferred_element_type=F32)).astype(dq_ref.dtype)
                dk_s[s_prev, cols] += lax.dot_general(dspb, qh, tn, preferred_element_type=F32)
                dk_s[s_cur, cols] += lax.dot_general(dscb, qh, tn, preferred_element_type=F32)
                dv_s[s_prev, cols] += lax.dot_general(pp.astype(_CD), dob, tn, preferred_element_type=F32)
                dv_s[s_cur, cols] += lax.dot_general(pc.astype(_CD), dob, tn, preferred_element_type=F32)
        dk_ref[...] = dk_s[DIL_W:, :].astype(dk_ref.dtype)
        dv_ref[...] = dv_s[DIL_W:, :].astype(dv_ref.dtype)
        dk_s[CHUNK:, :] = dk_s[:DIL_W, :]
        dv_s[CHUNK:, :] = dv_s[:DIL_W, :]

    cur = pl.BlockSpec((CHUNK, DIL_OUT), lambda s: (nchunk - 1 - s, 0))
    prev = pl.BlockSpec((DIL_W, DIL_OUT), lambda s: (jnp.maximum(4 * (nchunk - 1 - s) - 1, 0), 0))
    shp = jax.ShapeDtypeStruct((S, DIL_OUT), _CD)
    return _pcall(body, name=name, grid=(nchunk,), in_specs=[cur, cur, cur, prev, prev, cur, cur, cur, cur],
                  out_specs=[cur, cur, cur], out_shape=[shp, shp, shp],
                  scratch_shapes=[pltpu.VMEM((CHUNK + DIL_W, DIL_OUT), F32), pltpu.VMEM((CHUNK + DIL_W, DIL_OUT), F32)],
                  compiler_params=_params("arbitrary"))(q, k, v, k, v, o, lse, do, dlse)


def _dil_mix_fwd(os_, ls_, *, name, tm=512):
    S, W = os_[0].shape

    def body(o0, o1, o2, l0, l1, l2, out_ref):
        ls = [l0[...], l1[...], l2[...]]
        m = jnp.maximum(jnp.maximum(ls[0], ls[1]), ls[2])
        es = [jnp.exp(l - m) for l in ls]
        den = es[0] + es[1] + es[2]
        out_ref[...] = ((es[0] * o0[...] + es[1] * o1[...] + es[2] * o2[...]) / den).astype(out_ref.dtype)

    row = pl.BlockSpec((tm, W), lambda i: (i, 0))
    return _pcall(body, name=name, grid=(S // tm,), in_specs=[row] * 6, out_specs=row,
                  out_shape=jax.ShapeDtypeStruct((S, W), _CD), compiler_params=_params("parallel"))(*os_, *ls_)


def _dil_mix_bwd(doa, os_, ls_, *, name, tm=512, after=None):
    S, W = doa.shape

    def body(d_ref, o0, o1, o2, l0, l1, l2, do0, do1, do2, dl0, dl1, dl2):
        dv = d_ref[...]
        ls = [l0[...], l1[...], l2[...]]
        m = jnp.maximum(jnp.maximum(ls[0], ls[1]), ls[2])
        es = [jnp.exp(l - m) for l in ls]
        den = es[0] + es[1] + es[2]
        al = [e / den for e in es]
        da = [_head_sum(dv * o[...]) for o in (o0, o1, o2)]
        mean = al[0] * da[0] + al[1] * da[1] + al[2] * da[2]
        for a, d_, do_ref, dl_ref in zip(al, da, (do0, do1, do2), (dl0, dl1, dl2)):
            do_ref[...] = a * dv
            dl_ref[...] = a * (d_ - mean)

    row = pl.BlockSpec((tm, W), lambda i: (i, 0))
    shp = jax.ShapeDtypeStruct((S, W), F32)
    return _pcall(body, after, name=name, grid=(S // tm,), in_specs=[row] * 7, out_specs=[row] * 6, out_shape=[shp] * 6,
                  compiler_params=_params("parallel"))(doa, *os_, *ls_)


FOX_T = 512


PACK = 2 * HEAD_DIM
HEAD_PAIRS = N_FOX_HEADS // 2
FOX_HPS = 8
Q_BLOCK0 = (3 * DIL_WIDTH) // PACK
K_BLOCK0 = (3 * DIL_WIDTH + FOX_WIDTH) // PACK
V_BLOCK0 = (3 * DIL_WIDTH + 2 * FOX_WIDTH) // PACK


def _pieces(x):
    hi = x.astype(jnp.bfloat16).astype(F32)
    r = x - hi
    mid = r.astype(jnp.bfloat16).astype(F32)
    lo = (r - mid).astype(jnp.bfloat16).astype(F32)
    return [hi, mid, lo]


def _extras(first, second, rows):
    lane = lax.broadcasted_iota(jnp.int32, (rows, HEAD_DIM), 1)
    out = jnp.zeros((rows, HEAD_DIM), F32)
    for idx, val in enumerate(list(first) + list(second)):
        out = jnp.where(lane == idx, val, out)
    return out


def _head_column(c, h):
    lane = lax.broadcasted_iota(jnp.int32, c.shape, 1)
    return jnp.sum(jnp.where(lane == h, c, 0.0), axis=1, keepdims=True)


ONES3 = [1.0, 1.0, 1.0]
ZEROS3 = [0.0, 0.0, 0.0]


def _fox_pack_fwd(qkv, c, *, name, tm=512):
    S = qkv.shape[0]

    def body(q_ref, k_ref, v_ref, c_ref, qo_ref, ko_ref, vo_ref):
        hp = pl.program_id(1)
        cv = c_ref[...]
        for hh in range(2):
            ch = _pieces(_head_column(cv, 2 * hp + hh))
            src = slice(hh * HEAD_DIM, (hh + 1) * HEAD_DIM)
            lo = slice(hh * PACK, hh * PACK + HEAD_DIM)
            hi = slice(hh * PACK + HEAD_DIM, (hh + 1) * PACK)
            qo_ref[:, lo] = (q_ref[:, src].astype(F32) * ATTN_SCALE).astype(qo_ref.dtype)
            qo_ref[:, hi] = _extras(ch, ONES3, tm).astype(qo_ref.dtype)
            ko_ref[:, lo] = k_ref[:, src]
            ko_ref[:, hi] = _extras(ONES3, [-p for p in ch], tm).astype(ko_ref.dtype)
            vo_ref[:, lo] = v_ref[:, src]
            vo_ref[:, hi] = _extras(ONES3, ZEROS3, tm).astype(vo_ref.dtype)

    def src(block0):
        return pl.BlockSpec((tm, PACK), lambda i, hp: (i, block0 + hp))

    out = pl.BlockSpec((tm, 2 * PACK), lambda i, hp: (i, hp))
    shp = jax.ShapeDtypeStruct((S, N_FOX_HEADS * PACK), _CD)
    return _pcall(body, name=name, grid=(S // tm, HEAD_PAIRS),
                  in_specs=[src(Q_BLOCK0), src(K_BLOCK0), src(V_BLOCK0), pl.BlockSpec((tm, PACK), lambda i, hp: (i, 0))],
                  out_specs=[out, out, out], out_shape=[shp, shp, shp],
                  compiler_params=_params("parallel", "parallel"))(qkv, qkv, qkv, c)


def _fox_fwd(qp, kp, vp, *, name):
    S = qp.shape[0]
    nt = S // FOX_T
    nt_dims = (((1,), (1,)), ((), ()))
    tn_dims = (((0,), (0,)), ((), ()))

    def body(i_tab, j_tab, q_ref, k_ref, v_ref, o_ref, l_ref, m_s, acc_s):
        t = pl.program_id(1)
        i, j = i_tab[t], j_tab[t]

        @pl.when(j == 0)
        def _():
            m_s[...] = jnp.full((FOX_HPS, 1, FOX_T), NEG_INF, F32)
            acc_s[...] = jnp.zeros((FOX_HPS, PACK, FOX_T), F32)

        def tile(diagonal):
            for hh in range(FOX_HPS):
                cols = slice(hh * PACK, (hh + 1) * PACK)
                st = lax.dot_general(k_ref[:, cols], q_ref[:, cols], nt_dims, preferred_element_type=F32)
                if diagonal:
                    key = lax.broadcasted_iota(jnp.int32, (FOX_T, FOX_T), 0)
                    qry = lax.broadcasted_iota(jnp.int32, (FOX_T, FOX_T), 1)
                    st = jnp.where(key <= qry, st, NEG_INF)
                m_old = m_s[hh]
                m_new = jnp.maximum(m_old, jnp.max(st, axis=0, keepdims=True))
                pt = jnp.exp(st - m_new)
                acc_s[hh] = jnp.exp(m_old - m_new) * acc_s[hh] + lax.dot_general(
                    v_ref[:, cols], pt.astype(_CD), tn_dims, preferred_element_type=F32)
                m_s[hh] = m_new

        @pl.when(j < i)
        def _():
            tile(False)

        @pl.when(j == i)
        def _():
            tile(True)
            for hh in range(FOX_HPS):
                acc = acc_s[hh]
                den = acc[HEAD_DIM:HEAD_DIM + 1, :]
                cols = slice(hh * HEAD_DIM, (hh + 1) * HEAD_DIM)
                o_ref[:, cols] = (acc[:HEAD_DIM, :] / den).T
                l_ref[:, cols] = jnp.broadcast_to(m_s[hh] + jnp.log(den), (HEAD_DIM, FOX_T)).T

    pairs = [(i, j) for i in range(nt) for j in range(i + 1)]
    i_tab = jnp.asarray([p[0] for p in pairs], jnp.int32)
    j_tab = jnp.asarray([p[1] for p in pairs], jnp.int32)
    qs = pl.BlockSpec((FOX_T, FOX_HPS * PACK), lambda hp, t, it, jt: (it[t], hp))
    ks = pl.BlockSpec((FOX_T, FOX_HPS * PACK), lambda hp, t, it, jt: (jt[t], hp))
    os_ = pl.BlockSpec((FOX_T, FOX_HPS * HEAD_DIM), lambda hp, t, it, jt: (it[t], hp))
    shp = jax.ShapeDtypeStruct((S, FOX_WIDTH), F32)
    grid_spec = pltpu.PrefetchScalarGridSpec(
        num_scalar_prefetch=2, grid=(N_FOX_HEADS // FOX_HPS, len(pairs)), in_specs=[qs, ks, ks], out_specs=[os_, os_],
        scratch_shapes=[pltpu.VMEM((FOX_HPS, 1, FOX_T), F32), pltpu.VMEM((FOX_HPS, PACK, FOX_T), F32)])
    return _pcall(body, name=name, grid_spec=grid_spec, out_shape=[shp, shp],
                  compiler_params=_params("parallel", "arbitrary"))(i_tab, j_tab, qp, kp, vp)


def _fox_pack_bwd(qkv, c, o, lse, do, *, name, tm=512, after=None):
    S = qkv.shape[0]

    def body(q_ref, c_ref, o_ref, l_ref, do_ref, qo_ref, do_out_ref):
        hp = pl.program_id(1)
        cv = c_ref[...]
        for hh in range(2):
            src = slice(hh * HEAD_DIM, (hh + 1) * HEAD_DIM)
            lo = slice(hh * PACK, hh * PACK + HEAD_DIM)
            hi = slice(hh * PACK + HEAD_DIM, (hh + 1) * PACK)
            shift = _head_column(cv, 2 * hp + hh) - l_ref[:, hh * HEAD_DIM:hh * HEAD_DIM + 1]
            dov = do_ref[:, src]
            dsum = jnp.sum(dov * o_ref[:, src], axis=-1, keepdims=True)
            qo_ref[:, lo] = (q_ref[:, src].astype(F32) * ATTN_SCALE).astype(qo_ref.dtype)
            qo_ref[:, hi] = _extras(_pieces(shift), ONES3, tm).astype(qo_ref.dtype)
            do_out_ref[:, lo] = dov.astype(do_out_ref.dtype)
            do_out_ref[:, hi] = _extras(_pieces(-dsum), ZEROS3, tm).astype(do_out_ref.dtype)

    pair = pl.BlockSpec((tm, PACK), lambda i, hp: (i, hp))
    out = pl.BlockSpec((tm, 2 * PACK), lambda i, hp: (i, hp))
    shp = jax.ShapeDtypeStruct((S, N_FOX_HEADS * PACK), _CD)
    return _pcall(body, after, name=name, grid=(S // tm, HEAD_PAIRS),
                  in_specs=[pl.BlockSpec((tm, PACK), lambda i, hp: (i, Q_BLOCK0 + hp)),
                            pl.BlockSpec((tm, PACK), lambda i, hp: (i, 0)), pair, pair, pair],
                  out_specs=[out, out], out_shape=[shp, shp],
                  compiler_params=_params("parallel", "parallel"))(qkv, c, o, lse, do)


def _fox_bwd(qp, kp, vp, dop, *, name):
    S = qp.shape[0]
    nt = S // FOX_T
    nt_dims = (((1,), (1,)), ((), ()))
    tn_dims = (((0,), (0,)), ((), ()))

    def body(i_tab, j_tab, q_ref, k_ref, v_ref, do_ref, dq_ref, dk_ref, dv_ref, dc_ref, dr_ref,
             dq_s, dk_s, dv_s, dc_s, dr_s):
        t = pl.program_id(1)
        i, j = i_tab[t], j_tab[t]

        @pl.when(t == 0)
        def _():
            dq_s[...] = jnp.zeros((S, FOX_HPS * PACK), F32)
            dr_s[...] = jnp.zeros((FOX_HPS, 1, S), F32)

        @pl.when(i == j)
        def _():
            dk_s[...] = jnp.zeros((FOX_T, FOX_HPS * PACK), F32)
            dv_s[...] = jnp.zeros((FOX_T, FOX_HPS * PACK), F32)
            dc_s[...] = jnp.zeros((FOX_HPS, FOX_T, 1), F32)

        def tile(diagonal):
            rows = pl.ds(pl.multiple_of(i * FOX_T, FOX_T), FOX_T)
            for hh in range(FOX_HPS):
                cols = slice(hh * PACK, (hh + 1) * PACK)
                qv, kv, vv, dov = q_ref[:, cols], k_ref[:, cols], v_ref[:, cols], do_ref[:, cols]
                pt = jnp.exp(lax.dot_general(kv, qv, nt_dims, preferred_element_type=F32))
                if diagonal:
                    key = lax.broadcasted_iota(jnp.int32, (FOX_T, FOX_T), 0)
                    qry = lax.broadcasted_iota(jnp.int32, (FOX_T, FOX_T), 1)
                    pt = jnp.where(key <= qry, pt, 0.0)
                dst = pt * lax.dot_general(vv, dov, nt_dims, preferred_element_type=F32)
                dsb = dst.astype(_CD)
                dc_s[hh] += jnp.sum(dst, axis=1, keepdims=True)
                dr_s[hh, :, rows] += jnp.sum(dst, axis=0, keepdims=True)
                dv_s[:, cols] += jnp.dot(pt.astype(_CD), dov, preferred_element_type=F32)
                dk_s[:, cols] += jnp.dot(dsb, qv, preferred_element_type=F32)
                dq_s[rows, cols] += lax.dot_general(dsb, kv, tn_dims, preferred_element_type=F32)

        @pl.when(i > j)
        def _():
            tile(False)

        @pl.when(i == j)
        def _():
            tile(True)

        @pl.when(i == nt - 1)
        def _():
            for hh in range(FOX_HPS):
                src = slice(hh * PACK, hh * PACK + HEAD_DIM)
                dst_cols = slice(hh * HEAD_DIM, (hh + 1) * HEAD_DIM)
                dk_ref[:, dst_cols] = dk_s[:, src].astype(dk_ref.dtype)
                dv_ref[:, dst_cols] = dv_s[:, src].astype(dv_ref.dtype)
                dc_ref[:, dst_cols] = jnp.broadcast_to(dc_s[hh], (FOX_T, HEAD_DIM))

        @pl.when(t == len(pairs) - 1)
        def _():
            for hh in range(FOX_HPS):
                dq_ref[:, hh * HEAD_DIM:(hh + 1) * HEAD_DIM] = (
                    dq_s[:, hh * PACK:hh * PACK + HEAD_DIM] * ATTN_SCALE).astype(dq_ref.dtype)
            dr_ref[...] = dr_s[...]

    pairs = [(i, j) for j in range(nt) for i in range(j, nt)]
    i_tab = jnp.asarray([p[0] for p in pairs], jnp.int32)
    j_tab = jnp.asarray([p[1] for p in pairs], jnp.int32)
    wide, narrow = FOX_HPS * PACK, FOX_HPS * HEAD_DIM
    qs = pl.BlockSpec((FOX_T, wide), lambda hp, t, it, jt: (it[t], hp))
    ks = pl.BlockSpec((FOX_T, wide), lambda hp, t, it, jt: (jt[t], hp))
    whole = pl.BlockSpec((S, narrow), lambda hp, t, it, jt: (0, hp))
    cs = pl.BlockSpec((FOX_T, narrow), lambda hp, t, it, jt: (jt[t], hp))
    rs = pl.BlockSpec((FOX_HPS, 1, S), lambda hp, t, it, jt: (hp, 0, 0))
    shp = jax.ShapeDtypeStruct((S, FOX_WIDTH), _CD)
    grid_spec = pltpu.PrefetchScalarGridSpec(
        num_scalar_prefetch=2, grid=(N_FOX_HEADS // FOX_HPS, len(pairs)), in_specs=[qs, ks, ks, qs],
        out_specs=[whole, cs, cs, cs, rs],
        scratch_shapes=[pltpu.VMEM((S, wide), F32), pltpu.VMEM((FOX_T, wide), F32),
                        pltpu.VMEM((FOX_T, wide), F32), pltpu.VMEM((FOX_HPS, FOX_T, 1), F32),
                        pltpu.VMEM((FOX_HPS, 1, S), F32)])
    return _pcall(body, name=name, grid_spec=grid_spec,
                  out_shape=[shp, shp, shp, jax.ShapeDtypeStruct((S, FOX_WIDTH), F32),
                             jax.ShapeDtypeStruct((N_FOX_HEADS, 1, S), F32)],
                  compiler_params=_params("parallel", "arbitrary"))(i_tab, j_tab, qp, kp, vp, dop)


def _dedilate(t, d):
    if d == 1:
        return t
    S, C = t.shape
    return t.reshape(S // d, d, C).transpose(1, 0, 2).reshape(S, C)


def _redilate(t, d):
    if d == 1:
        return t
    S, C = t.shape
    return t.reshape(d, S // d, C).transpose(1, 0, 2).reshape(S, C)


def _layer_step(x, tgt, w, p, late_weights=None, grad_sink=None, after=None, first_weights=None):
    S = x.shape[0]
    after_norm, after_proj = after if after is not None else (None, None)
    h = _rms_fwd(x, p["norm_mix_g"], name="rms_mix", after=after_norm)
    if first_weights is not None:
        w = {**w, **first_weights(h)}
    qkv = _mm(h, w["qkv"], name="proj_qkv", out_dtype=_CD, tn=768, tm=2048, after=after_proj)
    zf = _mm(h, w["f"], name="proj_f")
    gl = _mm(h, w["g"], name="proj_gate", tn=1024)

    dil_q, dil_k, dil_v = [], [], []
    dil_o, dil_l = [], []
    for g, (_, d) in enumerate(DIL_PAIRS):
        qg = _dedilate(qkv[:, g * DIL_OUT:(g + 1) * DIL_OUT], d)
        kg = _dedilate(qkv[:, DIL_WIDTH + g * DIL_OUT:DIL_WIDTH + (g + 1) * DIL_OUT], d)
        vg = _dedilate(qkv[:, 2 * DIL_WIDTH + g * DIL_OUT:2 * DIL_WIDTH + (g + 1) * DIL_OUT], d)
        og, lg = _dil_fwd(qg, kg, vg, g, name=f"dil_fwd{g}")
        dil_q.append(qg), dil_k.append(kg), dil_v.append(vg)
        dil_o.append(_redilate(og, d)), dil_l.append(_redilate(lg, d))
    o_a = _dil_mix_fwd(dil_o, dil_l, name="dil_mix")

    c = _fox_cumsum(zf, p["b_fgt"], name="fox_cumsum")
    fqp, fkp, fvp = _fox_pack_fwd(qkv, c, name="fox_pack")
    o_b, flse = _fox_fwd(fqp, fkp, fvp, name="fox_fwd")

    if late_weights is not None:
        w = {**w, **late_weights(o_b)}
    y_a = _mm(o_a, w["dil_out"], name="y_a", tn=1024, out_dtype=_CD)
    y_b = _mm(o_b, w["fox_out"], name="y_b", tn=1024, out_dtype=_CD)
    merged = _gate_fwd(gl, p["b_gate"], y_a, y_b, name="gate_fwd")
    x1 = _mm(merged, w["out"], name="mix_out", add=x)

    h2 = _rms_fwd(x1, p["norm_ffn_g"], name="rms_ffn")
    gu = _mm(h2, w["ffn_in"], name="ffn_in", tn=1408, b_blocks=True, out_dtype=_CD, tm=2048)
    act = _swiglu_fwd(gu, name="swiglu")
    x2 = _mm(act, w["ffn_down"], name="ffn_down", add=x1, tk=2816)

    loss, dx2, dg_final = _loss_head(x2, p["norm_final_g"], tgt, name="loss_head")

    dact = _mm(dx2, w["ffn_down"], name="d_act", tb=True, tn=1408, out_dtype=_CD)
    gw_ffn_down = _mm(act, dx2, name="gw_ffn_down", ta=True, out_dtype=_CD, tm=1408)
    dgu = _swiglu_bwd(dact, gu, name="swiglu_bwd")
    dh2 = _mm(dgu, w["ffn_in"], name="d_h2", tb=True, tk=1408, b_blocks=True, tm=2048)
    gw_ffn_in = _mm(h2, dgu, name="gw_ffn_in", ta=True, out_dtype=_CD, tn=1408, out_blocks=1408)
    sink = grad_sink if grad_sink is not None else (lambda group, grads: None)
    tok = sink("ffn", dict(ffn_in=gw_ffn_in, ffn_down=gw_ffn_down))
    dx1, dg_ffn = _rms_bwd(x1, p["norm_ffn_g"], dh2, dx2, name="rms_ffn_bwd", after=tok)

    dmerged = _mm(dx1, w["out"], name="d_merged", tb=True, out_dtype=_CD)
    gw_out = _mm(merged, dx1, name="gw_out", ta=True, out_dtype=_CD)
    dy_a, dy_b, dgl, db_gate = _gate_bwd(dmerged, gl, p["b_gate"], y_a, y_b, name="gate_bwd")
    do_a = _mm(dy_a, w["dil_out"], name="d_o_a", tb=True)
    gw_dil_out = _mm(o_a, dy_a, name="gw_dil_out", ta=True, out_dtype=_CD, tn=1024)
    do_b = _mm(dy_b, w["fox_out"], name="d_o_b", tb=True)
    gw_fox_out = _mm(o_b, dy_b, name="gw_fox_out", ta=True, out_dtype=_CD, tn=1024)
    tok = sink("mix", dict(dil_out=gw_dil_out, fox_out=gw_fox_out, out=gw_out))

    bqp, bdop = _fox_pack_bwd(qkv, c, o_b, flse, do_b, name="fox_pack_bwd", after=tok)
    dqp, dkp, dvp, dck, dcq = _fox_bwd(bqp, fkp, fvp, bdop, name="fox_bwd")
    dc = dcq[:, 0, :].T - dck.reshape(S, N_FOX_HEADS, HEAD_DIM)[:, :, 0]
    dc = jnp.pad(dc, ((0, 0), (0, F_PAD - N_FOX_HEADS)))
    dzf, db_fgt = _fox_cumsum_bwd(dc, zf, p["b_fgt"], name="fox_cumsum_bwd")

    douts = _dil_mix_bwd(do_a, dil_o, dil_l, name="dil_mix_bwd", after=tok)
    dqs, dks, dvs = [], [], []
    for g, (_, d) in enumerate(DIL_PAIRS):
        dq, dk, dv = _dil_bwd(dil_q[g], dil_k[g], dil_v[g], _dedilate(dil_o[g], d), _dedilate(dil_l[g], d),
                              _dedilate(douts[g], d), _dedilate(douts[3 + g], d), g, name=f"dil_bwd{g}")
        dqs.append(_redilate(dq, d)), dks.append(_redilate(dk, d)), dvs.append(_redilate(dv, d))
    dqkv = jnp.concatenate(dqs + dks + dvs + [dqp, dkp, dvp], axis=1)

    gw_qkv = _mm(h, dqkv, name="gw_qkv", ta=True, out_dtype=_CD, tn=768)
    gw_g = _mm(h, dgl, name="gw_gate", ta=True, out_dtype=_CD)
    gw_f = _mm(h, dzf, name="gw_f", ta=True, out_dtype=_CD)
    tok = sink("in", dict(qkv=gw_qkv, f=gw_f, g=gw_g))
    dh = _mm(dqkv, w["qkv"], name="d_h_qkv", tb=True, tk=1920, tm=2048, after=tok)
    dh = _mm(dgl, w["g"], name="d_h_gate", tb=True, add=dh)
    dh = _mm(dzf, w["f"], name="d_h_f", tb=True, add=dh)
    dx, dg_mix = _rms_bwd(x, p["norm_mix_g"], dh, dx1, name="rms_mix_bwd")

    gw = dict(qkv=gw_qkv, f=gw_f, g=gw_g, dil_out=gw_dil_out, fox_out=gw_fox_out, out=gw_out, ffn_in=gw_ffn_in,
              ffn_down=gw_ffn_down)
    small = dict(norm_mix_g=dg_mix, b_fgt=db_fgt, b_gate=db_gate, norm_ffn_g=dg_ffn, norm_final_g=dg_final)
    return loss, dx, gw, small


def _position():
    return lax.axis_index("x"), lax.axis_index("y"), lax.axis_index("c")


def _other_chips(x, y):
    return [(1 - x, y), (x, 1 - y), (1 - x, 1 - y)]


ROW_TILE = 16


def _row_chunks(rows, want=4):
    n = want
    while n > 1 and rows % (n * ROW_TILE):
        n //= 2
    return n


SEM_SPEC = pl.BlockSpec(memory_space=pltpu.SEMAPHORE)
ANY_SPEC = pl.BlockSpec(memory_space=pl.ANY)
DATAFLOW = pltpu.SideEffectType.DATAFLOW_SIDE_EFFECTING


def _in_hbm(a):
    return pltpu.with_memory_space_constraint(a, pltpu.HBM)


def _split_copy_start(srcs, land_shapes, copies, after, *, name):
    n, m = len(srcs), len(land_shapes)

    def body(*refs):
        src_refs, land_refs = refs[:n], refs[n:n + m]
        send_sems, recv_sems = refs[n + m + 1], refs[n + m + 2]
        token = refs[-1]
        x, y, c = _position()
        for k, (src, dst, peer) in enumerate(copies(x, y, c, src_refs, land_refs)):
            pltpu.make_async_remote_copy(src_ref=src, dst_ref=dst, send_sem=send_sems.at[k], recv_sem=recv_sems.at[k],
                                         device_id=peer, device_id_type=MESH).start()
        token[...] = jnp.zeros_like(token)

    lands = [lax.empty(s.shape, s.dtype) for s in land_shapes]
    count = len(copies(0, 0, 0, srcs, lands))
    out = _pcall(
        body, name=name,
        out_shape=(pltpu.SemaphoreType.DMA((count,)), pltpu.SemaphoreType.DMA((count,)),
                   *[pltpu.HBM(s.shape, s.dtype) for s in srcs], *[pltpu.HBM(s.shape, s.dtype) for s in land_shapes],
                   jax.ShapeDtypeStruct((8, 128), F32)),
        in_specs=[HBM_SPEC] * (n + m) + [ANY_SPEC],
        out_specs=(SEM_SPEC, SEM_SPEC, *[HBM_SPEC] * (n + m), pl.BlockSpec(memory_space=pltpu.VMEM)),
        input_output_aliases={k: 2 + k for k in range(n + m)},
        compiler_params=pltpu.CompilerParams(has_side_effects=DATAFLOW),
    )(*[_in_hbm(s) for s in srcs], *[_in_hbm(l) for l in lands], after)
    return out[0], out[1], list(out[2:2 + n]), list(out[2 + n:2 + n + m]), out[-1]


def _split_copy_wait(send_sems, recv_sems, srcs, lands, copies, after, *, name):
    n, m = len(srcs), len(lands)

    def body(*refs):
        src_refs, land_refs = refs[:n], refs[n:n + m]
        send, recv = refs[n + m], refs[n + m + 1]
        x, y, c = _position()
        for k, (src, dst, peer) in enumerate(copies(x, y, c, src_refs, land_refs)):
            cp = pltpu.make_async_remote_copy(src_ref=src, dst_ref=dst, send_sem=send.at[k], recv_sem=recv.at[k],
                                              device_id=peer, device_id_type=MESH)
            cp.wait_send()
            cp.wait_recv()

    afters = list(after) if isinstance(after, (list, tuple)) else [after]
    out = _pcall(
        body, name=name,
        out_shape=tuple(pltpu.HBM(s.shape, s.dtype) for s in list(srcs) + list(lands)),
        in_specs=[HBM_SPEC] * (n + m) + [SEM_SPEC, SEM_SPEC] + [ANY_SPEC] * len(afters),
        out_specs=tuple([HBM_SPEC] * (n + m)),
        input_output_aliases={k: k for k in range(n + m)},
        compiler_params=pltpu.CompilerParams(has_side_effects=DATAFLOW),
    )(*srcs, *lands, send_sems, recv_sems, *afters)
    return list(out[:n]), list(out[n:])


def _gather_copies(x, y, c, shard_refs, land_refs):
    out = []
    for s, l in zip(shard_refs, land_refs):
        half = s.shape[0] // 2
        nq = _row_chunks(half)
        for cx, cy in _other_chips(x, y):
            for q in range(nq):
                rows = pl.ds(c * half + q * (half // nq), half // nq)
                out.append((s.at[rows, :], l.at[2 * x + y, rows, :], (cx, cy, c)))
    return out


def _scatter_copies(x, y, c, part_refs, land_refs):
    out = []
    for p, l in zip(part_refs, land_refs):
        nq = _row_chunks(p.shape[1])
        for r, (cx, cy) in enumerate(_other_chips(x, y)):
            for q in range(nq):
                rows = pl.ds(q * (p.shape[1] // nq), p.shape[1] // nq)
                out.append((p.at[2 * cx + cy, rows, :], l.at[r, rows, :], (cx, cy, c)))
    return out


def _forward_halves(lands, *, name):
    n = len(lands)

    def body(*refs):
        ins = refs[:n]
        send_sems, recv_sems = refs[2 * n:]
        x, y, c = _position()
        copies = []
        for w in range(n):
            half = ins[w].shape[1] // 2
            for r, (cx, cy) in enumerate(_other_chips(x, y)):
                blk = ins[w].at[2 * cx + cy, pl.ds(c * half, half), :]
                cp = pltpu.make_async_remote_copy(src_ref=blk, dst_ref=blk, send_sem=send_sems.at[w, r],
                                                  recv_sem=recv_sems.at[w, r], device_id=(x, y, 1 - c),
                                                  device_id_type=MESH)
                cp.start()
                copies.append(cp)
        for w in range(n):
            half = ins[w].shape[1] // 2
            for r, (cx, cy) in enumerate(_other_chips(x, y)):
                blk = ins[w].at[2 * cx + cy, pl.ds((1 - c) * half, half), :]
                pltpu.make_async_remote_copy(src_ref=blk, dst_ref=blk, send_sem=send_sems.at[w, r],
                                             recv_sem=recv_sems.at[w, r], device_id=(x, y, 1 - c),
                                             device_id_type=MESH).wait_recv()
        for cp in copies:
            cp.wait_send()

    return _pcall(
        body, name=name, in_specs=[HBM_SPEC] * n, out_specs=[HBM_SPEC] * n,
        out_shape=[jax.ShapeDtypeStruct(l.shape, l.dtype) for l in lands],
        input_output_aliases={k: k for k in range(n)},
        scratch_shapes=[pltpu.SemaphoreType.DMA((n, 3)), pltpu.SemaphoreType.DMA((n, 3))],
    )(*lands)


def _swap_halves(grads, name="swap_halves"):
    n = len(grads)

    def body(*refs):
        ins, outs = refs[:n], refs[n:2 * n]
        send_sems, recv_sems = refs[2 * n:]
        x, y, c = _position()
        copies = []
        for w in range(n):
            half = ins[w].shape[1] // 2
            cp = pltpu.make_async_remote_copy(
                src_ref=ins[w].at[:, pl.ds((1 - c) * half, half), :], dst_ref=outs[w], send_sem=send_sems.at[w],
                recv_sem=recv_sems.at[w], device_id=(x, y, 1 - c), device_id_type=MESH)
            cp.start()
            copies.append(cp)
        for cp in copies:
            cp.wait()

    return _pcall(
        body, name=name, in_specs=[HBM_SPEC] * n, out_specs=[HBM_SPEC] * n,
        out_shape=[jax.ShapeDtypeStruct((4, g.shape[1] // 2, g.shape[2]), g.dtype) for g in grads],
        scratch_shapes=[pltpu.SemaphoreType.DMA((n,)), pltpu.SemaphoreType.DMA((n,))],
    )(*grads)


def _share_halves(halves):
    n = len(halves)

    def body(*refs):
        ins, outs = refs[:n], refs[n:2 * n]
        send_sems, recv_sems = refs[2 * n:]
        x, y, c = _position()
        copies = []
        for w in range(n):
            cp = pltpu.make_async_remote_copy(src_ref=ins[w], dst_ref=outs[w], send_sem=send_sems.at[w],
                                              recv_sem=recv_sems.at[w], device_id=(x, y, 1 - c), device_id_type=MESH)
            cp.start()
            copies.append(cp)
        for cp in copies:
            cp.wait()

    return _pcall(
        body, name="share_halves", in_specs=[HBM_SPEC] * n, out_specs=[HBM_SPEC] * n,
        out_shape=[jax.ShapeDtypeStruct(h.shape, h.dtype) for h in halves],
        scratch_shapes=[pltpu.SemaphoreType.DMA((n,)), pltpu.SemaphoreType.DMA((n,))],
    )(*halves)


def _sum_small(part):
    rows, width = part.shape

    def body(x_ref, out_ref, all_ref, send_sems, recv_sems):
        x, y, c = _position()
        me, sibling = (x, y, c), (x, y, 1 - c)
        chips = _other_chips(x, y)

        def block(px, py, pc):
            return all_ref.at[pl.ds((4 * px + 2 * py + pc) * rows, rows), :]

        def copy(k, blk, to, src=None):
            return pltpu.make_async_remote_copy(
                src_ref=block(*blk) if src is None else src, dst_ref=block(*blk), send_sem=send_sems.at[k],
                recv_sem=recv_sems.at[k], device_id=to, device_id_type=MESH)

        all_ref[pl.ds((4 * x + 2 * y + c) * rows, rows), :] = x_ref[...]
        first = [copy(0, me, sibling, src=x_ref)]
        first += [copy(1 + j, me, (*chip, c), src=x_ref) for j, chip in enumerate(chips)]
        for cp in first:
            cp.start()
        passed = [copy(4 + j, (*chip, c), sibling) for j, chip in enumerate(chips)]
        for j, chip in enumerate(chips):
            copy(1 + j, (*chip, c), me).wait_recv()
            passed[j].start()
        copy(0, sibling, me).wait_recv()
        for j, chip in enumerate(chips):
            copy(4 + j, (*chip, 1 - c), me).wait_recv()
        for cp in first + passed:
            cp.wait_send()
        total = all_ref[0:rows, :]
        for d in range(1, 8):
            total = total + all_ref[d * rows:(d + 1) * rows, :]
        out_ref[...] = total

    vm = pl.BlockSpec(memory_space=pltpu.VMEM)
    return _pcall(
        body, name="sum_small", in_specs=[vm], out_specs=vm, out_shape=jax.ShapeDtypeStruct((rows, width), F32),
        scratch_shapes=[pltpu.VMEM((8 * rows, width), F32), pltpu.SemaphoreType.DMA((7,)), pltpu.SemaphoreType.DMA((7,))],
    )(part)


def _row_tile(R, C, itemsize=4, budget=1 << 20):
    for t in (512, 256, 128, 64, 32, 16, 8):
        if R % t == 0 and t * C * itemsize <= budget:
            return t
    return R


def _add_halves(g, recv, c, *, name):
    _, R, C = g.shape
    half = R // 2
    t = _row_tile(half, C)
    nb = half // t

    def body(c_ref, g_ref, r_ref, o_ref):
        o_ref[...] = (g_ref[...].astype(F32) + r_ref[...].astype(F32)).astype(o_ref.dtype)

    grid_spec = pltpu.PrefetchScalarGridSpec(
        num_scalar_prefetch=1, grid=(4, nb),
        in_specs=[pl.BlockSpec((1, t, C), lambda k, i, cr: (k, cr[0] * nb + i, 0)),
                  pl.BlockSpec((1, t, C), lambda k, i, cr: (k, i, 0))],
        out_specs=pl.BlockSpec((1, t, C), lambda k, i, cr: (k, i, 0)))
    return _pcall(body, name=name, grid_spec=grid_spec, out_shape=jax.ShapeDtypeStruct((4, half, C), g.dtype),
                  compiler_params=_params("parallel", "parallel"))(c, g, recv)


def _add_owners(mine, recv, *, name):
    half, C = mine.shape
    t = _row_tile(half, C)

    def body(m_ref, r_ref, o_ref):
        o_ref[...] = ((m_ref[...].astype(F32) + r_ref[0].astype(F32)) + r_ref[1].astype(F32)) + r_ref[2].astype(F32)

    return _pcall(body, name=name, grid=(half // t,),
                  in_specs=[pl.BlockSpec((t, C), lambda i: (i, 0)), pl.BlockSpec((3, t, C), lambda i: (0, i, 0))],
                  out_specs=pl.BlockSpec((t, C), lambda i: (i, 0)), out_shape=jax.ShapeDtypeStruct((half, C), F32),
                  compiler_params=_params("parallel"))(mine, recv)


def _adamw(w, g, m, v, *, name):
    R, C = w.shape
    t = _row_tile(R, C)
    c1 = 1.0 - ADAM_B1 ** ADAM_STEP
    c2 = 1.0 - ADAM_B2 ** ADAM_STEP

    def body(w_ref, g_ref, m_ref, v_ref, d_ref, nm_ref, nv_ref):
        gv = g_ref[...]
        mn = ADAM_B1 * m_ref[...] + (1.0 - ADAM_B1) * gv
        vn = ADAM_B2 * v_ref[...] + (1.0 - ADAM_B2) * (gv * gv)
        d_ref[...] = -ADAM_LR * ((mn / c1) / (jnp.sqrt(vn / c2) + ADAM_EPS) + ADAM_WD * w_ref[...])
        nm_ref[...] = mn
        nv_ref[...] = vn

    blk = pl.BlockSpec((t, C), lambda i: (i, 0))
    shp = jax.ShapeDtypeStruct((R, C), F32)
    return _pcall(body, name=name, grid=(R // t,), in_specs=[blk] * 4, out_specs=[blk] * 3, out_shape=[shp] * 3,
                  compiler_params=_params("parallel"))(w, g, m, v)


BIG = ("w_in", "w_dil_out", "w_fox_out", "w_out", "w_ffn_in", "w_ffn_down")
SMALL = ("norm_mix_g", "b_fgt", "b_gate", "norm_ffn_g", "norm_final_g")
ORDER = ("norm_mix_g", "w_in", "b_fgt", "b_gate", "w_dil_out", "w_fox_out", "w_out", "norm_ffn_g", "w_ffn_in",
         "w_ffn_down", "norm_final_g")
SMALL_ROWS = {"norm_mix_g": (0, 1), "b_gate": (1, 3), "norm_ffn_g": (3, 4), "norm_final_g": (4, 5), "b_fgt": (5, 6)}


def _columns_to_blocks(full, ncol):
    K = full.shape[0]
    return full.reshape(K, 4, ncol).transpose(1, 0, 2)


def _blocks_to_columns(blocks):
    n, K, ncol = blocks.shape
    return blocks.transpose(1, 0, 2).reshape(K, n * ncol)


def kernel(x, norm_mix_g, w_in, b_fgt, b_gate, w_dil_out, w_fox_out, w_out, norm_ffn_g, w_ffn_in, w_ffn_down, norm_final_g, loss_target, m_norm_mix_g, m_w_in, m_b_fgt, m_b_gate, m_w_dil_out, m_w_fox_out, m_w_out, m_norm_ffn_g, m_w_ffn_in, m_w_ffn_down, m_norm_final_g, v_norm_mix_g, v_w_in, v_b_fgt, v_b_gate, v_w_dil_out, v_w_fox_out, v_w_out, v_norm_ffn_g, v_w_ffn_in, v_w_ffn_down, v_norm_final_g):
    weights = dict(norm_mix_g=norm_mix_g, w_in=w_in, b_fgt=b_fgt, b_gate=b_gate, w_dil_out=w_dil_out,
                   w_fox_out=w_fox_out, w_out=w_out, norm_ffn_g=norm_ffn_g, w_ffn_in=w_ffn_in, w_ffn_down=w_ffn_down,
                   norm_final_g=norm_final_g)
    m_in = dict(norm_mix_g=m_norm_mix_g, w_in=m_w_in, b_fgt=m_b_fgt, b_gate=m_b_gate, w_dil_out=m_w_dil_out,
                w_fox_out=m_w_fox_out, w_out=m_w_out, norm_ffn_g=m_norm_ffn_g, w_ffn_in=m_w_ffn_in,
                w_ffn_down=m_w_ffn_down, norm_final_g=m_norm_final_g)
    v_in = dict(norm_mix_g=v_norm_mix_g, w_in=v_w_in, b_fgt=v_b_fgt, b_gate=v_b_gate, w_dil_out=v_w_dil_out,
                w_fox_out=v_w_fox_out, w_out=v_w_out, norm_ffn_g=v_norm_ffn_g, w_ffn_in=v_w_ffn_in,
                w_ffn_down=v_w_ffn_down, norm_final_g=v_norm_final_g)
    c = lax.axis_index("c")
    chip = 2 * lax.axis_index("x") + lax.axis_index("y")

    shards = {n: weights[n][0].astype(_CD) for n in BIG}
    in_shape = jax.ShapeDtypeStruct((4,) + shards["w_in"].shape, _CD)
    send_i, recv_i, in_src, in_land, token_in = _split_copy_start(
        [shards["w_in"]], [in_shape], _gather_copies, norm_mix_g, name="gather_in_start")
    late = BIG[1:]
    send_g, recv_g, late_src, late_land, token = _split_copy_start(
        [shards[n] for n in late], [jax.ShapeDtypeStruct((4,) + shards[n].shape, _CD) for n in late],
        _gather_copies, token_in, name="gather_late_start")
    adam_in = [t[0] + token_in[0, 0] for t in (w_in, m_w_in, v_w_in)]
    p = dict(norm_mix_g=norm_mix_g, b_fgt=jnp.pad(b_fgt, ((0, 0), (0, F_PAD - N_FOX_HEADS))), b_gate=b_gate,
             norm_ffn_g=norm_ffn_g, norm_final_g=norm_final_g.reshape(1, D_MODEL))

    def first_weights(after):
        own, lands = _split_copy_wait(send_i, recv_i, in_src, in_land, _gather_copies, [after] + adam_in,
                                      name="gather_in_wait")
        (g_in,) = _forward_halves(lands, name="gather_in_forward")
        full_in = _blocks_to_columns(lax.dynamic_update_index_in_dim(g_in, own[0], chip, 0))
        o3 = QKV_COLS
        o4 = o3 + N_FOX_HEADS
        return dict(qkv=full_in[:, :o3], f=jnp.pad(full_in[:, o3:o4], ((0, 0), (0, F_PAD - N_FOX_HEADS))),
                    g=full_in[:, o4:])

    def late_weights(after):
        own, lands = _split_copy_wait(send_g, recv_g, late_src, late_land, _gather_copies, after,
                                      name="gather_late_wait")
        lands = _forward_halves(lands, name="gather_late_forward")
        g_dil, g_fox, g_out, g_ffn_in, g_ffn_down = [
            lax.dynamic_update_index_in_dim(l, s, chip, 0) for l, s in zip(lands, own)]
        return dict(dil_out=_blocks_to_columns(g_dil), fox_out=_blocks_to_columns(g_fox),
                    out=g_out.reshape(D_MODEL, D_MODEL), ffn_in=g_ffn_in,
                    ffn_down=g_ffn_down.reshape(D_FF, D_MODEL))

    c_arr = jnp.reshape(c, (1,)).astype(jnp.int32)

    def to_blocks(n, full):
        shape = weights[n].shape
        if full.ndim == 3:
            return full
        if n in ("w_out", "w_ffn_down"):
            return full.reshape(4, shape[1], shape[2])
        return _columns_to_blocks(full, shape[2])

    def pair_sums(group, named):
        names = list(named)
        blocks = [to_blocks(n, named[n]) for n in names]
        from_sibling = _swap_halves(blocks, name=f"swap_halves_{group}")
        return [_add_halves(b, r, c_arr, name=f"add_halves_{n}") for b, r, n in zip(blocks, from_sibling, names)]

    in_flight = {}

    def grad_sink(group, gw):
        if group == "in":
            named = {"w_in": jnp.concatenate([gw["qkv"], gw["f"][:, :N_FOX_HEADS], gw["g"]], axis=1)}
        else:
            named = {"w_" + k: v for k, v in gw.items()}
        sums = pair_sums(group, named)
        started = _split_copy_start(sums, [jax.ShapeDtypeStruct((3,) + s.shape[1:], s.dtype) for s in sums],
                                    _scatter_copies, next(iter(gw.values())), name=f"scatter_{group}_start")
        in_flight[group] = (list(named), started)
        return started[-1]

    loss_part, grad_x, gw, small = _layer_step(x[0], loss_target[0], {}, p, late_weights, grad_sink,
                                               (token_in, token), first_weights)

    def owner_sums(names, sums, from_chips):
        return {n: _add_owners(lax.dynamic_index_in_dim(s, chip, 0, keepdims=False), r, name=f"add_owners_{n}")
                for n, s, r in zip(names, sums, from_chips)}

    halves = {}
    for group, (names, (send_s, recv_s, srcs, lands, _)) in in_flight.items():
        sums, from_chips = _split_copy_wait(send_s, recv_s, srcs, lands, _scatter_copies, grad_x,
                                            name=f"scatter_{group}_wait")
        halves.update(owner_sums(names, sums, from_chips))
    halves = [halves[n] for n in BIG]
    grads = {}
    for n, own, other in zip(BIG, halves, _share_halves(halves)):
        pair = jnp.stack([own, other])
        grads[n] = jnp.where(c == 0, pair, pair[::-1]).reshape(2 * own.shape[0], own.shape[1])

    packed = jnp.concatenate([
        small["norm_mix_g"], small["b_gate"].reshape(2, D_MODEL), small["norm_ffn_g"], small["norm_final_g"],
        jnp.pad(small["b_fgt"], ((0, 0), (0, D_MODEL - F_PAD))), jnp.zeros((2, D_MODEL), F32)], axis=0)
    summed = _sum_small(packed)
    for n in SMALL:
        lo, hi = SMALL_ROWS[n]
        grads[n] = summed[lo:hi].reshape(1, -1)[:, :weights[n].size]

    loss = lax.psum(loss_part[0, 0], ("x", "y", "c"))

    out_g, out_d, out_m, out_v = {}, {}, {}, {}
    for n in ORDER:
        shape = weights[n].shape
        two_d = shape[1:] if len(shape) == 3 else (1, weights[n].size)
        g2 = grads[n].reshape(two_d)
        wmv = adam_in if n == "w_in" else [t.reshape(two_d) for t in (weights[n], m_in[n], v_in[n])]
        d2, m2, v2 = _adamw(wmv[0], g2, wmv[1], wmv[2], name=f"adamw_{n}")
        out_g[n], out_d[n], out_m[n], out_v[n] = (g2.reshape(shape), d2.reshape(shape), m2.reshape(shape),
                                                  v2.reshape(shape))
    return (loss, grad_x[None], *[out_g[n] for n in ORDER], *[out_d[n] for n in ORDER],
            *[out_m[n] for n in ORDER], *[out_v[n] for n in ORDER])
```

```python
import numpy as np
import jax
import jax.numpy as jnp
from jax import lax
from jax.experimental import pallas as pl
from jax.experimental.pallas import tpu as pltpu

F32 = jnp.float32
_CD = jnp.bfloat16

D_MODEL = 1024
HEAD_DIM = 64
DIL_PAIRS = ((128, 1), (512, 4), (2048, 16))
N_DIL_GROUPS = 3
DIL_HEADS = 4
DIL_W = 128
DIL_OUT = DIL_HEADS * HEAD_DIM
DIL_WIDTH = N_DIL_GROUPS * DIL_OUT
N_FOX_HEADS = 8
FOX_WIDTH = N_FOX_HEADS * HEAD_DIM
D_FF = 2816
QKV_COLS = 3 * DIL_WIDTH + 3 * FOX_WIDTH
F_PAD = 128
RMS_EPS = 1e-6
NEG_INF = -1e30
ATTN_SCALE = HEAD_DIM ** -0.5
ADAM_LR, ADAM_B1, ADAM_B2, ADAM_EPS, ADAM_WD, ADAM_STEP = 0.001, 0.9, 0.999, 1e-08, 0.01, 10

VMEM_LIMIT = 48 * 1024 * 1024
LANES = 128
MESH = pl.DeviceIdType.MESH
HBM_SPEC = pl.BlockSpec(memory_space=pltpu.HBM)


def _pcall(body, after=None, **kw):
    if after is None:
        return pl.pallas_call(body, **kw)
    n_in = len(kw["in_specs"])
    kw["in_specs"] = list(kw["in_specs"]) + [pl.BlockSpec(memory_space=pl.ANY)]

    def tied(*refs):
        return body(*refs[:n_in], *refs[n_in + 1:])

    call = pl.pallas_call(tied, **kw)
    return lambda *args: call(*args, after)


def _params(*sem):
    return pltpu.CompilerParams(dimension_semantics=sem, vmem_limit_bytes=VMEM_LIMIT)


def _pick(dim, pref):
    t = (min(pref, dim) // 128) * 128
    while t >= 128:
        if dim % t == 0:
            return t
        t -= 128
    return dim


def _mm(a, b, *, name, ta=False, tb=False, out_dtype=F32, add=None, tm=1024, tn=512, tk=2048, after=None,
        b_blocks=False, out_blocks=None):
    if ta:
        K, M = a.shape
    else:
        M, K = a.shape
    b_rows, b_cols = (b.shape[1], b.shape[0] * b.shape[2]) if b_blocks else b.shape
    if tb:
        N, K2 = b_rows, b_cols
    else:
        K2, N = b_rows, b_cols
    assert K == K2, (a.shape, b.shape)
    shard = b.shape[2] if b_blocks else None
    tm = _pick(M, tm)
    tn = _pick(shard if (b_blocks and not tb) else (out_blocks or N), tn)
    tk = _pick(shard if (b_blocks and tb) else K, tk)
    nk = K // tk
    dn = (((0 if ta else 1,), (1 if tb else 0,)), ((), ()))
    has_add = add is not None
    assert not (has_add and out_blocks)

    def body(*refs):
        a_ref, b_ref = refs[0], refs[1]
        add_ref = refs[2] if has_add else None
        o_ref = refs[3] if has_add else refs[2]
        bv = b_ref[0] if b_blocks else b_ref[...]
        p = lax.dot_general(a_ref[...].astype(_CD), bv.astype(_CD), dn, preferred_element_type=F32)

        def finish(r):
            if has_add:
                r = r + add_ref[...]
            if out_blocks:
                o_ref[0] = r.astype(out_dtype)
            else:
                o_ref[...] = r.astype(out_dtype)

        if nk == 1:
            finish(p)
        else:
            acc_ref = refs[-1]
            k = pl.program_id(2)

            @pl.when(k == 0)
            def _():
                acc_ref[...] = p

            @pl.when(k > 0)
            def _():
                acc_ref[...] += p

            @pl.when(k == nk - 1)
            def _():
                finish(acc_ref[...])

    a_spec = pl.BlockSpec((tk, tm), lambda i, j, k: (k, i)) if ta else pl.BlockSpec((tm, tk), lambda i, j, k: (i, k))
    if b_blocks and tb:
        per = shard // tk
        b_spec = pl.BlockSpec((1, tn, tk), lambda i, j, k: (k // per, j, k % per))
    elif b_blocks:
        per = shard // tn
        b_spec = pl.BlockSpec((1, tk, tn), lambda i, j, k: (j // per, k, j % per))
    else:
        b_spec = pl.BlockSpec((tn, tk), lambda i, j, k: (j, k)) if tb else pl.BlockSpec((tk, tn), lambda i, j, k: (k, j))
    if out_blocks:
        oper = out_blocks // tn
        o_spec = pl.BlockSpec((1, tm, tn), lambda i, j, k: (j // oper, i, j % oper))
        out_shape = jax.ShapeDtypeStruct((N // out_blocks, M, out_blocks), out_dtype)
    else:
        o_spec = pl.BlockSpec((tm, tn), lambda i, j, k: (i, j))
        out_shape = jax.ShapeDtypeStruct((M, N), out_dtype)
    in_specs = [a_spec, b_spec] + ([o_spec] if has_add else [])
    args = (a, b) + ((add,) if has_add else ())
    return _pcall(
        body, after, name=name, grid=(M // tm, N // tn, nk), in_specs=in_specs, out_specs=o_spec,
        out_shape=out_shape,
        scratch_shapes=[pltpu.VMEM((tm, tn), F32)] if nk > 1 else [],
        compiler_params=_params("parallel", "parallel", "arbitrary"),
    )(*args)


def _rms_fwd(x, g, *, name, tm=512, after=None):
    S, D = x.shape

    def body(x_ref, g_ref, h_ref):
        xv = x_ref[...]
        r = lax.rsqrt(jnp.mean(xv * xv, axis=-1, keepdims=True) + RMS_EPS)
        h_ref[...] = ((xv * r) * g_ref[...]).astype(h_ref.dtype)

    row = pl.BlockSpec((tm, D), lambda i: (i, 0))
    return _pcall(body, after, name=name, grid=(S // tm,), in_specs=[row, pl.BlockSpec((1, D), lambda i: (0, 0))],
                  out_specs=row, out_shape=jax.ShapeDtypeStruct((S, D), _CD), compiler_params=_params("parallel"))(x, g)


def _rms_bwd(x, g, dh, dres, *, name, tm=512, after=None):
    S, D = x.shape

    def body(x_ref, g_ref, dh_ref, dres_ref, dx_ref, dg_ref):
        xv = x_ref[...]
        r = lax.rsqrt(jnp.mean(xv * xv, axis=-1, keepdims=True) + RMS_EPS)
        xh = xv * r
        dhv = dh_ref[...]
        dxh = dhv * g_ref[...]
        dx_ref[...] = dres_ref[...] + r * (dxh - xh * jnp.mean(dxh * xh, axis=-1, keepdims=True))
        part = jnp.sum(dhv * xh, axis=0, keepdims=True)

        @pl.when(pl.program_id(0) == 0)
        def _():
            dg_ref[...] = part

        @pl.when(pl.program_id(0) > 0)
        def _():
            dg_ref[...] += part

    row = pl.BlockSpec((tm, D), lambda i: (i, 0))
    vec = pl.BlockSpec((1, D), lambda i: (0, 0))
    return _pcall(body, after, name=name, grid=(S // tm,), in_specs=[row, vec, row, row], out_specs=[row, vec],
                  out_shape=[jax.ShapeDtypeStruct((S, D), F32), jax.ShapeDtypeStruct((1, D), F32)],
                  compiler_params=_params("arbitrary"))(x, g, dh, dres)


def _loss_head(x, g, tgt, *, name, tm=512):
    S, D = x.shape

    def body(x_ref, g_ref, t_ref, loss_ref, dx_ref, dg_ref):
        xv = x_ref[...]
        gv = g_ref[...]
        r = lax.rsqrt(jnp.mean(xv * xv, axis=-1, keepdims=True) + RMS_EPS)
        xh = xv * r
        err = xh * gv - t_ref[...]
        lpart = 0.5 * jnp.sum(jnp.mean(err * err, axis=-1, keepdims=True), axis=0, keepdims=True)
        dy = err * (1.0 / D)
        dxh = dy * gv
        dx_ref[...] = r * (dxh - xh * jnp.mean(dxh * xh, axis=-1, keepdims=True))
        gpart = jnp.sum(dy * xh, axis=0, keepdims=True)

        @pl.when(pl.program_id(0) == 0)
        def _():
            loss_ref[...] = lpart
            dg_ref[...] = gpart

        @pl.when(pl.program_id(0) > 0)
        def _():
            loss_ref[...] += lpart
            dg_ref[...] += gpart

    row = pl.BlockSpec((tm, D), lambda i: (i, 0))
    vec = pl.BlockSpec((1, D), lambda i: (0, 0))
    one = pl.BlockSpec((1, 1), lambda i: (0, 0))
    return _pcall(body, name=name, grid=(S // tm,), in_specs=[row, vec, row], out_specs=[one, row, vec],
                  out_shape=[jax.ShapeDtypeStruct((1, 1), F32), jax.ShapeDtypeStruct((S, D), F32),
                             jax.ShapeDtypeStruct((1, D), F32)],
                  compiler_params=_params("arbitrary"))(x, g, tgt)


def _sigmoid(z):
    return 1.0 / (1.0 + jnp.exp(-z))


def _gate_fwd(gl, bg, ya, yb, *, name, tm=512):
    S, D = ya.shape

    def body(za_ref, zb_ref, ba_ref, bb_ref, ya_ref, yb_ref, o_ref):
        ga = _sigmoid(za_ref[...].astype(F32) + ba_ref[...])
        gb = _sigmoid(zb_ref[...].astype(F32) + bb_ref[...])
        o_ref[...] = (ga * ya_ref[...].astype(F32) + gb * yb_ref[...].astype(F32)).astype(o_ref.dtype)

    lo = pl.BlockSpec((tm, D), lambda i: (i, 0))
    hi = pl.BlockSpec((tm, D), lambda i: (i, 1))
    vlo = pl.BlockSpec((1, D), lambda i: (0, 0))
    vhi = pl.BlockSpec((1, D), lambda i: (0, 1))
    return _pcall(body, name=name, grid=(S // tm,), in_specs=[lo, hi, vlo, vhi, lo, lo], out_specs=lo,
                  out_shape=jax.ShapeDtypeStruct((S, D), _CD), compiler_params=_params("parallel"))(gl, gl, bg, bg, ya, yb)


def _gate_bwd(dm, gl, bg, ya, yb, *, name, tm=256):
    S, D = ya.shape

    def body(dm_ref, za_ref, zb_ref, ba_ref, bb_ref, ya_ref, yb_ref, dya_ref, dyb_ref, dgl_ref, dbg_ref):
        dmv = dm_ref[...].astype(F32)
        ga = _sigmoid(za_ref[...].astype(F32) + ba_ref[...])
        gb = _sigmoid(zb_ref[...].astype(F32) + bb_ref[...])
        dya_ref[...] = (dmv * ga).astype(dya_ref.dtype)
        dyb_ref[...] = (dmv * gb).astype(dyb_ref.dtype)
        dza = dmv * ya_ref[...].astype(F32) * ga * (1.0 - ga)
        dzb = dmv * yb_ref[...].astype(F32) * gb * (1.0 - gb)
        dgl_ref[:, :D] = dza.astype(dgl_ref.dtype)
        dgl_ref[:, D:] = dzb.astype(dgl_ref.dtype)
        pa = jnp.sum(dza, axis=0, keepdims=True)
        pb = jnp.sum(dzb, axis=0, keepdims=True)

        @pl.when(pl.program_id(0) == 0)
        def _():
            dbg_ref[:, :D] = pa
            dbg_ref[:, D:] = pb

        @pl.when(pl.program_id(0) > 0)
        def _():
            dbg_ref[:, :D] += pa
            dbg_ref[:, D:] += pb

    lo = pl.BlockSpec((tm, D), lambda i: (i, 0))
    hi = pl.BlockSpec((tm, D), lambda i: (i, 1))
    vlo = pl.BlockSpec((1, D), lambda i: (0, 0))
    vhi = pl.BlockSpec((1, D), lambda i: (0, 1))
    wide = pl.BlockSpec((tm, 2 * D), lambda i: (i, 0))
    vwide = pl.BlockSpec((1, 2 * D), lambda i: (0, 0))
    return _pcall(body, name=name, grid=(S // tm,), in_specs=[lo, lo, hi, vlo, vhi, lo, lo],
                  out_specs=[lo, lo, wide, vwide],
                  out_shape=[jax.ShapeDtypeStruct((S, D), _CD), jax.ShapeDtypeStruct((S, D), _CD),
                             jax.ShapeDtypeStruct((S, 2 * D), _CD), jax.ShapeDtypeStruct((1, 2 * D), F32)],
                  compiler_params=_params("arbitrary"))(dm, gl, gl, bg, bg, ya, yb)


def _swiglu_fwd(gu, *, name, tm=256):
    S, F2 = gu.shape
    F = F2 // 2

    def body(g_ref, u_ref, o_ref):
        gv = g_ref[...].astype(F32)
        o_ref[...] = (gv * _sigmoid(gv) * u_ref[...].astype(F32)).astype(o_ref.dtype)

    lo = pl.BlockSpec((tm, F), lambda i: (i, 0))
    hi = pl.BlockSpec((tm, F), lambda i: (i, 1))
    return _pcall(body, name=name, grid=(S // tm,), in_specs=[lo, hi], out_specs=lo,
                  out_shape=jax.ShapeDtypeStruct((S, F), _CD), compiler_params=_params("parallel"))(gu, gu)


def _swiglu_bwd(dact, gu, *, name, tm=256):
    S, F2 = gu.shape
    F = F2 // 2

    def body(d_ref, g_ref, u_ref, o_ref):
        dv = d_ref[...].astype(F32)
        gv = g_ref[...].astype(F32)
        sg = _sigmoid(gv)
        o_ref[:, :F] = (dv * u_ref[...].astype(F32) * (sg * (1.0 + gv * (1.0 - sg)))).astype(o_ref.dtype)
        o_ref[:, F:] = (dv * (gv * sg)).astype(o_ref.dtype)

    lo = pl.BlockSpec((tm, F), lambda i: (i, 0))
    hi = pl.BlockSpec((tm, F), lambda i: (i, 1))
    return _pcall(body, name=name, grid=(S // tm,), in_specs=[lo, lo, hi],
                  out_specs=pl.BlockSpec((tm, F2), lambda i: (i, 0)),
                  out_shape=jax.ShapeDtypeStruct((S, F2), _CD), compiler_params=_params("parallel"))(dact, gu, gu)


def _split3(x):
    hi = x.astype(jnp.bfloat16)
    r1 = x - hi.astype(F32)
    mid = r1.astype(jnp.bfloat16)
    lo = (r1 - mid.astype(F32)).astype(jnp.bfloat16)
    return hi, mid, lo


def _ones_dot_left(ones, x):
    return sum(jnp.dot(ones, p, preferred_element_type=F32) for p in _split3(x))


def _ones_dot_right(x, ones):
    return sum(jnp.dot(p, ones, preferred_element_type=F32) for p in _split3(x))


def _head_sum(x):
    n = x.shape[1]
    r = lax.broadcasted_iota(jnp.int32, (n, n), 0) // HEAD_DIM
    c = lax.broadcasted_iota(jnp.int32, (n, n), 1) // HEAD_DIM
    return _ones_dot_right(x, (r == c).astype(jnp.bfloat16))


def _log_sigmoid(z):
    e = jnp.exp(-jnp.abs(z))
    t = 1.0 + e
    log1p_e = jnp.where(t == 1.0, e, jnp.log(t) * (e / jnp.where(t == 1.0, 1.0, t - 1.0)))
    return jnp.minimum(z, 0.0) - log1p_e


def _fox_cumsum(zf, bf, *, name):
    S, W = zf.shape
    nb = S // 128

    def body(z_ref, b_ref, c_ref):
        tri = (lax.broadcasted_iota(jnp.int32, (128, 128), 0) >= lax.broadcasted_iota(jnp.int32, (128, 128), 1))
        tri = tri.astype(jnp.bfloat16)

        def step(i, carry):
            rows = pl.ds(pl.multiple_of(i * 128, 128), 128)
            lf = _log_sigmoid(z_ref[rows, :] + b_ref[...])
            cb = _ones_dot_left(tri, lf) + carry
            c_ref[rows, :] = cb
            return cb[127:128, :]

        lax.fori_loop(0, nb, step, jnp.zeros((1, W), F32))

    return _pcall(body, name=name, out_shape=jax.ShapeDtypeStruct((S, W), F32),
                  compiler_params=pltpu.CompilerParams(vmem_limit_bytes=VMEM_LIMIT))(zf, bf)


def _fox_cumsum_bwd(dc, zf, bf, *, name):
    S, W = zf.shape
    nb = S // 128

    def body(dc_ref, z_ref, b_ref, dz_ref, db_ref):
        tri = (lax.broadcasted_iota(jnp.int32, (128, 128), 0) <= lax.broadcasted_iota(jnp.int32, (128, 128), 1))
        tri = tri.astype(jnp.bfloat16)

        def step(k, carry):
            tail, acc = carry
            i = nb - 1 - k
            rows = pl.ds(pl.multiple_of(i * 128, 128), 128)
            dlf = _ones_dot_left(tri, dc_ref[rows, :]) + tail
            dz = dlf * _sigmoid(-(z_ref[rows, :] + b_ref[...]))
            dz_ref[rows, :] = dz
            return dlf[0:1, :], acc + jnp.sum(dz, axis=0, keepdims=True)

        _, acc = lax.fori_loop(0, nb, step, (jnp.zeros((1, W), F32), jnp.zeros((1, W), F32)))
        db_ref[...] = acc

    return _pcall(body, name=name,
                  out_shape=[jax.ShapeDtypeStruct((S, W), F32), jax.ShapeDtypeStruct((1, W), F32)],
                  compiler_params=pltpu.CompilerParams(vmem_limit_bytes=VMEM_LIMIT))(dc, zf, bf)


def _proj_dil(h, w_qkv, *, name, tm=1024):
    S, D = h.shape
    tn = DIL_WIDTH

    def body(a_ref, b_ref, *rest):
        outs, acc = rest[:N_DIL_GROUPS], rest[N_DIL_GROUPS]
        prod = jnp.dot(a_ref[...].astype(_CD), b_ref[...].astype(_CD), preferred_element_type=F32)
        for k in range(tn // LANES):
            acc[k] = prod[:, k * LANES:(k + 1) * LANES]
        for g, (_, d) in enumerate(DIL_PAIRS):
            for half in range(DIL_OUT // LANES):
                k = g * (DIL_OUT // LANES) + half
                cols = slice(half * LANES, (half + 1) * LANES)
                for r in range(d):
                    rows = pl.ds(r, tm // d, stride=d) if d > 1 else slice(None)
                    outs[g][0, r, :, cols] = acc[k, rows, :].astype(outs[g].dtype)

    out_specs = [pl.BlockSpec((1, d, tm // d, DIL_OUT), lambda i, j: (j, 0, i, 0)) for _, d in DIL_PAIRS]
    out_shape = [jax.ShapeDtypeStruct((3, d, S // d, DIL_OUT), _CD) for _, d in DIL_PAIRS]
    outs = _pcall(body, name=name, grid=(S // tm, 3),
                  in_specs=[pl.BlockSpec((tm, D), lambda i, j: (i, 0)), pl.BlockSpec((D, tn), lambda i, j: (0, j))],
                  out_specs=out_specs, out_shape=out_shape, scratch_shapes=[pltpu.VMEM((tn // LANES, tm, LANES), F32)],
                  compiler_params=_params("parallel", "arbitrary"))(h, w_qkv)
    return [o.reshape(3, S, DIL_OUT) for o in outs]


def _dil_start(block, S, dilation):
    sub = S // dilation
    u0 = block * DIL_W
    return (u0 % sub) * dilation + u0 // sub


def _dil_slopes(group):
    h = np.arange(1, N_DIL_GROUPS * DIL_HEADS + 1, dtype=np.float32)
    s = (np.float32(2.0) ** (np.float32(-8.0) * h / np.float32(N_DIL_GROUPS * DIL_HEADS))).astype(np.float32)
    return [float(v) for v in s.reshape(N_DIL_GROUPS, DIL_HEADS)[group]]


def _dil_tiles(i, n, blocks_per_seq):
    qi = lax.broadcasted_iota(jnp.int32, (DIL_W, DIL_W), 0)
    kj = lax.broadcasted_iota(jnp.int32, (DIL_W, DIL_W), 1)
    first = ((4 * n + i) % blocks_per_seq) == 0
    valid_prev = jnp.logical_and(kj >= qi, jnp.logical_not(first))
    valid_cur = kj <= qi
    rel_prev = (qi - kj + DIL_W).astype(F32)
    rel_cur = (qi - kj).astype(F32)
    return valid_prev, valid_cur, rel_prev, rel_cur


CHUNK = 4 * DIL_W


def _dil_rows(block, S, dilation):
    start = _dil_start(block, S, dilation)
    return pl.ds(start, DIL_W, stride=dilation) if dilation > 1 else pl.ds(start, DIL_W)


def SPLIT(S):
    return (DIL_OUT // LANES, S, LANES)


def _dil_fwd(qkv, group, *, name):
    S = qkv.shape[1]
    dilation = DIL_PAIRS[group][1]
    bps = (S // dilation) // DIL_W
    slopes = _dil_slopes(group)
    nt = (((1,), (1,)), ((), ()))

    def body(q_ref, k_ref, v_ref, kp_ref, vp_ref, on_ref, ln_ref, o_ref, l_ref):
        n = pl.program_id(0)
        for i in range(4):
            valid_prev, valid_cur, rel_prev, rel_cur = _dil_tiles(i, n, bps)
            rows = slice(i * DIL_W, (i + 1) * DIL_W)
            prow = slice((i - 1) * DIL_W, i * DIL_W)
            for h in range(DIL_HEADS):
                cols = slice(h * HEAD_DIM, (h + 1) * HEAD_DIM)
                qh = q_ref[rows, cols]
                kc, vc = k_ref[rows, cols], v_ref[rows, cols]
                kp = kp_ref[:, cols] if i == 0 else k_ref[prow, cols]
                vp = vp_ref[:, cols] if i == 0 else v_ref[prow, cols]
                sl = slopes[h] * dilation
                sp = lax.dot_general(qh, kp, nt, preferred_element_type=F32) * ATTN_SCALE - sl * rel_prev
                sc = lax.dot_general(qh, kc, nt, preferred_element_type=F32) * ATTN_SCALE - sl * rel_cur
                sp = jnp.where(valid_prev, sp, NEG_INF)
                sc = jnp.where(valid_cur, sc, NEG_INF)
                m = jnp.maximum(jnp.max(sp, axis=-1, keepdims=True), jnp.max(sc, axis=-1, keepdims=True))
                pp, pc = jnp.exp(sp - m), jnp.exp(sc - m)
                den = jnp.sum(pp, axis=-1, keepdims=True) + jnp.sum(pc, axis=-1, keepdims=True)
                acc = (jnp.dot(pp.astype(_CD), vp, preferred_element_type=F32)
                       + jnp.dot(pc.astype(_CD), vc, preferred_element_type=F32))
                o_ref[rows, cols] = acc / den
                l_ref[rows, cols] = jnp.broadcast_to(m + jnp.log(den), (DIL_W, HEAD_DIM))
        for i in range(4):
            rows = slice(i * DIL_W, (i + 1) * DIL_W)
            nat = _dil_rows(4 * n + i, S, dilation)
            for half in range(DIL_OUT // LANES):
                cols = slice(half * LANES, (half + 1) * LANES)
                on_ref[half, nat, :] = o_ref[rows, cols]
                ln_ref[half, nat, :] = l_ref[rows, cols]

    def cur(which):
        return pl.BlockSpec((None, CHUNK, DIL_OUT), lambda n: (which, n, 0))

    def prev(which):
        return pl.BlockSpec((None, DIL_W, DIL_OUT), lambda n: (which, jnp.maximum(4 * n - 1, 0), 0))

    whole = pl.BlockSpec(SPLIT(S), lambda n: (0, 0, 0))
    return _pcall(body, name=name, grid=(S // CHUNK,), in_specs=[cur(0), cur(1), cur(2), prev(1), prev(2)],
                  out_specs=[whole, whole],
                  out_shape=[jax.ShapeDtypeStruct(SPLIT(S), F32), jax.ShapeDtypeStruct(SPLIT(S), F32)],
                  scratch_shapes=[pltpu.VMEM((CHUNK, DIL_OUT), F32), pltpu.VMEM((CHUNK, DIL_OUT), F32)],
                  compiler_params=_params("arbitrary"))(qkv, qkv, qkv, qkv, qkv)


def _dil_bwd(qkv, o, lse, do, dlse, group, *, name):
    S = qkv.shape[1]
    dilation = DIL_PAIRS[group][1]
    bps = (S // dilation) // DIL_W
    slopes = _dil_slopes(group)
    nchunk = S // CHUNK
    nt = (((1,), (1,)), ((), ()))
    tn = (((0,), (0,)), ((), ()))

    def body(q_ref, k_ref, v_ref, kp_ref, vp_ref, on_ref, ln_ref, don_ref, dln_ref, dq_ref, dk_ref, dv_ref,
             dk_s, dv_s, o_ref, l_ref, do_ref, dl_ref):
        step = pl.program_id(0)
        n = nchunk - 1 - step
        for i in range(4):
            rows = slice(i * DIL_W, (i + 1) * DIL_W)
            nat = _dil_rows(4 * n + i, S, dilation)
            for half in range(DIL_OUT // LANES):
                cols = slice(half * LANES, (half + 1) * LANES)
                o_ref[rows, cols] = on_ref[half, nat, :]
                l_ref[rows, cols] = ln_ref[half, nat, :]
                do_ref[rows, cols] = don_ref[half, nat, :]
                dl_ref[rows, cols] = dln_ref[half, nat, :]

        @pl.when(step == 0)
        def _():
            dk_s[CHUNK:, :] = jnp.zeros((DIL_W, DIL_OUT), F32)
            dv_s[CHUNK:, :] = jnp.zeros((DIL_W, DIL_OUT), F32)

        dk_s[:CHUNK, :] = jnp.zeros((CHUNK, DIL_OUT), F32)
        dv_s[:CHUNK, :] = jnp.zeros((CHUNK, DIL_OUT), F32)
        for i in range(4):
            valid_prev, valid_cur, rel_prev, rel_cur = _dil_tiles(i, n, bps)
            rows = slice(i * DIL_W, (i + 1) * DIL_W)
            prow = slice((i - 1) * DIL_W, i * DIL_W)
            s_prev = slice(i * DIL_W, (i + 1) * DIL_W)
            s_cur = slice((i + 1) * DIL_W, (i + 2) * DIL_W)
            for h in range(DIL_HEADS):
                cols = slice(h * HEAD_DIM, (h + 1) * HEAD_DIM)
                qh = q_ref[rows, cols]
                kc, vc = k_ref[rows, cols], v_ref[rows, cols]
                kp = kp_ref[:, cols] if i == 0 else k_ref[prow, cols]
                vp = vp_ref[:, cols] if i == 0 else v_ref[prow, cols]
                sl = slopes[h] * dilation
                lh = l_ref[rows, h * HEAD_DIM:h * HEAD_DIM + 1]
                sp = lax.dot_general(qh, kp, nt, preferred_element_type=F32) * ATTN_SCALE - sl * rel_prev
                sc = lax.dot_general(qh, kc, nt, preferred_element_type=F32) * ATTN_SCALE - sl * rel_cur
                pp = jnp.exp(jnp.where(valid_prev, sp, NEG_INF) - lh)
                pc = jnp.exp(jnp.where(valid_cur, sc, NEG_INF) - lh)
                doh = do_ref[rows, cols]
                dsum = jnp.sum(doh * o_ref[rows, cols], axis=-1, keepdims=True)
                shift = dl_ref[rows, h * HEAD_DIM:h * HEAD_DIM + 1] - dsum
                dob = doh.astype(_CD)
                dsp = pp * (lax.dot_general(dob, vp, nt, preferred_element_type=F32) + shift)
                dsc = pc * (lax.dot_general(dob, vc, nt, preferred_element_type=F32) + shift)
                dspb = (dsp * ATTN_SCALE).astype(_CD)
                dscb = (dsc * ATTN_SCALE).astype(_CD)
                dq_ref[rows, cols] = (jnp.dot(dspb, kp, preferred_element_type=F32)
                                      + jnp.dot(dscb, kc, preferred_element_type=F32)).astype(dq_ref.dtype)
                dk_s[s_prev, cols] += lax.dot_general(dspb, qh, tn, preferred_element_type=F32)
                dk_s[s_cur, cols] += lax.dot_general(dscb, qh, tn, preferred_element_type=F32)
                dv_s[s_prev, cols] += lax.dot_general(pp.astype(_CD), dob, tn, preferred_element_type=F32)
                dv_s[s_cur, cols] += lax.dot_general(pc.astype(_CD), dob, tn, preferred_element_type=F32)
        dk_ref[...] = dk_s[DIL_W:, :].astype(dk_ref.dtype)
        dv_ref[...] = dv_s[DIL_W:, :].astype(dv_ref.dtype)
        dk_s[CHUNK:, :] = dk_s[:DIL_W, :]
        dv_s[CHUNK:, :] = dv_s[:DIL_W, :]

    def cur(which):
        return pl.BlockSpec((None, CHUNK, DIL_OUT), lambda s: (which, nchunk - 1 - s, 0))

    def prev(which):
        return pl.BlockSpec((None, DIL_W, DIL_OUT), lambda s: (which, jnp.maximum(4 * (nchunk - 1 - s) - 1, 0), 0))

    whole = pl.BlockSpec(SPLIT(S), lambda s: (0, 0, 0))
    out = pl.BlockSpec((CHUNK, DIL_OUT), lambda s: (nchunk - 1 - s, 0))
    shp = jax.ShapeDtypeStruct((S, DIL_OUT), _CD)
    tile = pltpu.VMEM((CHUNK, DIL_OUT), F32)
    return _pcall(body, name=name, grid=(nchunk,),
                  in_specs=[cur(0), cur(1), cur(2), prev(1), prev(2), whole, whole, whole, whole],
                  out_specs=[out, out, out], out_shape=[shp, shp, shp],
                  scratch_shapes=[pltpu.VMEM((CHUNK + DIL_W, DIL_OUT), F32), pltpu.VMEM((CHUNK + DIL_W, DIL_OUT), F32),
                                  tile, tile, tile, tile],
                  compiler_params=_params("arbitrary"))(qkv, qkv, qkv, qkv, qkv, o, lse, do, dlse)


def _dil_mix_fwd(os_, ls_, *, name, tm=512):
    nh, S, _ = os_[0].shape

    def body(o0, o1, o2, l0, l1, l2, out_ref):
        for half in range(nh):
            ls = [l0[half], l1[half], l2[half]]
            m = jnp.maximum(jnp.maximum(ls[0], ls[1]), ls[2])
            es = [jnp.exp(l - m) for l in ls]
            den = es[0] + es[1] + es[2]
            mixed = (es[0] * o0[half] + es[1] * o1[half] + es[2] * o2[half]) / den
            out_ref[:, half * LANES:(half + 1) * LANES] = mixed.astype(out_ref.dtype)

    halves = pl.BlockSpec((nh, tm, LANES), lambda i: (0, i, 0))
    row = pl.BlockSpec((tm, nh * LANES), lambda i: (i, 0))
    return _pcall(body, name=name, grid=(S // tm,), in_specs=[halves] * 6, out_specs=row,
                  out_shape=jax.ShapeDtypeStruct((S, nh * LANES), _CD), compiler_params=_params("parallel"))(*os_, *ls_)


def _dil_mix_bwd(doa, os_, ls_, *, name, tm=512, after=None):
    nh, S, _ = os_[0].shape

    def body(d_ref, o0, o1, o2, l0, l1, l2, do0, do1, do2, dl0, dl1, dl2):
        for half in range(nh):
            dv = d_ref[:, half * LANES:(half + 1) * LANES]
            ls = [l0[half], l1[half], l2[half]]
            m = jnp.maximum(jnp.maximum(ls[0], ls[1]), ls[2])
            es = [jnp.exp(l - m) for l in ls]
            den = es[0] + es[1] + es[2]
            al = [e / den for e in es]
            da = [_head_sum(dv * o[half]) for o in (o0, o1, o2)]
            mean = al[0] * da[0] + al[1] * da[1] + al[2] * da[2]
            for a, d_, do_ref, dl_ref in zip(al, da, (do0, do1, do2), (dl0, dl1, dl2)):
                do_ref[half] = a * dv
                dl_ref[half] = a * (d_ - mean)

    halves = pl.BlockSpec((nh, tm, LANES), lambda i: (0, i, 0))
    row = pl.BlockSpec((tm, nh * LANES), lambda i: (i, 0))
    shp = jax.ShapeDtypeStruct((nh, S, LANES), F32)
    return _pcall(body, after, name=name, grid=(S // tm,), in_specs=[row] + [halves] * 6, out_specs=[halves] * 6,
                  out_shape=[shp] * 6, compiler_params=_params("parallel"))(doa, *os_, *ls_)


FOX_T = 512


PACK = 2 * HEAD_DIM
HEAD_PAIRS = N_FOX_HEADS // 2
FOX_HPS = 8
Q_BLOCK0 = 0
K_BLOCK0 = FOX_WIDTH // PACK
V_BLOCK0 = 2 * FOX_WIDTH // PACK


def _pieces(x):
    hi = x.astype(jnp.bfloat16).astype(F32)
    r = x - hi
    mid = r.astype(jnp.bfloat16).astype(F32)
    lo = (r - mid).astype(jnp.bfloat16).astype(F32)
    return [hi, mid, lo]


def _extras(first, second, rows):
    lane = lax.broadcasted_iota(jnp.int32, (rows, HEAD_DIM), 1)
    out = jnp.zeros((rows, HEAD_DIM), F32)
    for idx, val in enumerate(list(first) + list(second)):
        out = jnp.where(lane == idx, val, out)
    return out


def _head_column(c, h):
    lane = lax.broadcasted_iota(jnp.int32, c.shape, 1)
    return jnp.sum(jnp.where(lane == h, c, 0.0), axis=1, keepdims=True)


ONES3 = [1.0, 1.0, 1.0]
ZEROS3 = [0.0, 0.0, 0.0]


def _fox_pack_fwd(qkv, c, *, name, tm=512):
    S = qkv.shape[0]

    def body(q_ref, k_ref, v_ref, c_ref, qo_ref, ko_ref, vo_ref):
        hp = pl.program_id(1)
        cv = c_ref[...]
        for hh in range(2):
            ch = _pieces(_head_column(cv, 2 * hp + hh))
            src = slice(hh * HEAD_DIM, (hh + 1) * HEAD_DIM)
            lo = slice(hh * PACK, hh * PACK + HEAD_DIM)
            hi = slice(hh * PACK + HEAD_DIM, (hh + 1) * PACK)
            qo_ref[:, lo] = (q_ref[:, src].astype(F32) * ATTN_SCALE).astype(qo_ref.dtype)
            qo_ref[:, hi] = _extras(ch, ONES3, tm).astype(qo_ref.dtype)
            ko_ref[:, lo] = k_ref[:, src]
            ko_ref[:, hi] = _extras(ONES3, [-p for p in ch], tm).astype(ko_ref.dtype)
            vo_ref[:, lo] = v_ref[:, src]
            vo_ref[:, hi] = _extras(ONES3, ZEROS3, tm).astype(vo_ref.dtype)

    def src(block0):
        return pl.BlockSpec((tm, PACK), lambda i, hp: (i, block0 + hp))

    out = pl.BlockSpec((tm, 2 * PACK), lambda i, hp: (i, hp))
    shp = jax.ShapeDtypeStruct((S, N_FOX_HEADS * PACK), _CD)
    return _pcall(body, name=name, grid=(S // tm, HEAD_PAIRS),
                  in_specs=[src(Q_BLOCK0), src(K_BLOCK0), src(V_BLOCK0), pl.BlockSpec((tm, PACK), lambda i, hp: (i, 0))],
                  out_specs=[out, out, out], out_shape=[shp, shp, shp],
                  compiler_params=_params("parallel", "parallel"))(qkv, qkv, qkv, c)


def _fox_fwd(qp, kp, vp, *, name):
    S = qp.shape[0]
    nt = S // FOX_T
    nt_dims = (((1,), (1,)), ((), ()))
    tn_dims = (((0,), (0,)), ((), ()))

    def body(i_tab, j_tab, q_ref, k_ref, v_ref, o_ref, l_ref, m_s, acc_s):
        t = pl.program_id(1)
        i, j = i_tab[t], j_tab[t]

        @pl.when(j == 0)
        def _():
            m_s[...] = jnp.full((FOX_HPS, 1, FOX_T), NEG_INF, F32)
            acc_s[...] = jnp.zeros((FOX_HPS, PACK, FOX_T), F32)

        def tile(diagonal):
            for hh in range(FOX_HPS):
                cols = slice(hh * PACK, (hh + 1) * PACK)
                st = lax.dot_general(k_ref[:, cols], q_ref[:, cols], nt_dims, preferred_element_type=F32)
                if diagonal:
                    key = lax.broadcasted_iota(jnp.int32, (FOX_T, FOX_T), 0)
                    qry = lax.broadcasted_iota(jnp.int32, (FOX_T, FOX_T), 1)
                    st = jnp.where(key <= qry, st, NEG_INF)
                m_old = m_s[hh]
                m_new = jnp.maximum(m_old, jnp.max(st, axis=0, keepdims=True))
                pt = jnp.exp(st - m_new)
                acc_s[hh] = jnp.exp(m_old - m_new) * acc_s[hh] + lax.dot_general(
                    v_ref[:, cols], pt.astype(_CD), tn_dims, preferred_element_type=F32)
                m_s[hh] = m_new

        @pl.when(j < i)
        def _():
            tile(False)

        @pl.when(j == i)
        def _():
            tile(True)
            for hh in range(FOX_HPS):
                acc = acc_s[hh]
                den = acc[HEAD_DIM:HEAD_DIM + 1, :]
                cols = slice(hh * HEAD_DIM, (hh + 1) * HEAD_DIM)
                o_ref[:, cols] = (acc[:HEAD_DIM, :] / den).T
                l_ref[:, cols] = jnp.broadcast_to(m_s[hh] + jnp.log(den), (HEAD_DIM, FOX_T)).T

    pairs = [(i, j) for i in range(nt) for j in range(i + 1)]
    i_tab = jnp.asarray([p[0] for p in pairs], jnp.int32)
    j_tab = jnp.asarray([p[1] for p in pairs], jnp.int32)
    qs = pl.BlockSpec((FOX_T, FOX_HPS * PACK), lambda hp, t, it, jt: (it[t], hp))
    ks = pl.BlockSpec((FOX_T, FOX_HPS * PACK), lambda hp, t, it, jt: (jt[t], hp))
    os_ = pl.BlockSpec((FOX_T, FOX_HPS * HEAD_DIM), lambda hp, t, it, jt: (it[t], hp))
    shp = jax.ShapeDtypeStruct((S, FOX_WIDTH), F32)
    grid_spec = pltpu.PrefetchScalarGridSpec(
        num_scalar_prefetch=2, grid=(N_FOX_HEADS // FOX_HPS, len(pairs)), in_specs=[qs, ks, ks], out_specs=[os_, os_],
        scratch_shapes=[pltpu.VMEM((FOX_HPS, 1, FOX_T), F32), pltpu.VMEM((FOX_HPS, PACK, FOX_T), F32)])
    return _pcall(body, name=name, grid_spec=grid_spec, out_shape=[shp, shp],
                  compiler_params=_params("parallel", "arbitrary"))(i_tab, j_tab, qp, kp, vp)


def _fox_pack_bwd(qkv, c, o, lse, do, *, name, tm=512, after=None):
    S = qkv.shape[0]

    def body(q_ref, c_ref, o_ref, l_ref, do_ref, qo_ref, do_out_ref):
        hp = pl.program_id(1)
        cv = c_ref[...]
        for hh in range(2):
            src = slice(hh * HEAD_DIM, (hh + 1) * HEAD_DIM)
            lo = slice(hh * PACK, hh * PACK + HEAD_DIM)
            hi = slice(hh * PACK + HEAD_DIM, (hh + 1) * PACK)
            shift = _head_column(cv, 2 * hp + hh) - l_ref[:, hh * HEAD_DIM:hh * HEAD_DIM + 1]
            dov = do_ref[:, src]
            dsum = jnp.sum(dov * o_ref[:, src], axis=-1, keepdims=True)
            qo_ref[:, lo] = (q_ref[:, src].astype(F32) * ATTN_SCALE).astype(qo_ref.dtype)
            qo_ref[:, hi] = _extras(_pieces(shift), ONES3, tm).astype(qo_ref.dtype)
            do_out_ref[:, lo] = dov.astype(do_out_ref.dtype)
            do_out_ref[:, hi] = _extras(_pieces(-dsum), ZEROS3, tm).astype(do_out_ref.dtype)

    pair = pl.BlockSpec((tm, PACK), lambda i, hp: (i, hp))
    out = pl.BlockSpec((tm, 2 * PACK), lambda i, hp: (i, hp))
    shp = jax.ShapeDtypeStruct((S, N_FOX_HEADS * PACK), _CD)
    return _pcall(body, after, name=name, grid=(S // tm, HEAD_PAIRS),
                  in_specs=[pl.BlockSpec((tm, PACK), lambda i, hp: (i, Q_BLOCK0 + hp)),
                            pl.BlockSpec((tm, PACK), lambda i, hp: (i, 0)), pair, pair, pair],
                  out_specs=[out, out], out_shape=[shp, shp],
                  compiler_params=_params("parallel", "parallel"))(qkv, c, o, lse, do)


def _fox_bwd(qp, kp, vp, dop, *, name):
    S = qp.shape[0]
    nt = S // FOX_T
    nt_dims = (((1,), (1,)), ((), ()))
    tn_dims = (((0,), (0,)), ((), ()))

    def body(i_tab, j_tab, q_ref, k_ref, v_ref, do_ref, dq_ref, dk_ref, dv_ref, dc_ref, dr_ref,
             dq_s, dk_s, dv_s, dc_s, dr_s):
        t = pl.program_id(1)
        i, j = i_tab[t], j_tab[t]

        @pl.when(t == 0)
        def _():
            dq_s[...] = jnp.zeros((S, FOX_HPS * PACK), F32)
            dr_s[...] = jnp.zeros((FOX_HPS, 1, S), F32)

        @pl.when(i == j)
        def _():
            dk_s[...] = jnp.zeros((FOX_T, FOX_HPS * PACK), F32)
            dv_s[...] = jnp.zeros((FOX_T, FOX_HPS * PACK), F32)
            dc_s[...] = jnp.zeros((FOX_HPS, FOX_T, 1), F32)

        def tile(diagonal):
            rows = pl.ds(pl.multiple_of(i * FOX_T, FOX_T), FOX_T)
            for hh in range(FOX_HPS):
                cols = slice(hh * PACK, (hh + 1) * PACK)
                qv, kv, vv, dov = q_ref[:, cols], k_ref[:, cols], v_ref[:, cols], do_ref[:, cols]
                pt = jnp.exp(lax.dot_general(kv, qv, nt_dims, preferred_element_type=F32))
                if diagonal:
                    key = lax.broadcasted_iota(jnp.int32, (FOX_T, FOX_T), 0)
                    qry = lax.broadcasted_iota(jnp.int32, (FOX_T, FOX_T), 1)
                    pt = jnp.where(key <= qry, pt, 0.0)
                dst = pt * lax.dot_general(vv, dov, nt_dims, preferred_element_type=F32)
                dsb = dst.astype(_CD)
                dc_s[hh] += jnp.sum(dst, axis=1, keepdims=True)
                dr_s[hh, :, rows] += jnp.sum(dst, axis=0, keepdims=True)
                dv_s[:, cols] += jnp.dot(pt.astype(_CD), dov, preferred_element_type=F32)
                dk_s[:, cols] += jnp.dot(dsb, qv, preferred_element_type=F32)
                dq_s[rows, cols] += lax.dot_general(dsb, kv, tn_dims, preferred_element_type=F32)

        @pl.when(i > j)
        def _():
            tile(False)

        @pl.when(i == j)
        def _():
            tile(True)

        @pl.when(i == nt - 1)
        def _():
            for hh in range(FOX_HPS):
                src = slice(hh * PACK, hh * PACK + HEAD_DIM)
                dst_cols = slice(hh * HEAD_DIM, (hh + 1) * HEAD_DIM)
                dk_ref[:, dst_cols] = dk_s[:, src].astype(dk_ref.dtype)
                dv_ref[:, dst_cols] = dv_s[:, src].astype(dv_ref.dtype)
                dc_ref[:, dst_cols] = jnp.broadcast_to(dc_s[hh], (FOX_T, HEAD_DIM))

        @pl.when(t == len(pairs) - 1)
        def _():
            for hh in range(FOX_HPS):
                dq_ref[:, hh * HEAD_DIM:(hh + 1) * HEAD_DIM] = (
                    dq_s[:, hh * PACK:hh * PACK + HEAD_DIM] * ATTN_SCALE).astype(dq_ref.dtype)
            dr_ref[...] = dr_s[...]

    pairs = [(i, j) for j in range(nt) for i in range(j, nt)]
    i_tab = jnp.asarray([p[0] for p in pairs], jnp.int32)
    j_tab = jnp.asarray([p[1] for p in pairs], jnp.int32)
    wide, narrow = FOX_HPS * PACK, FOX_HPS * HEAD_DIM
    qs = pl.BlockSpec((FOX_T, wide), lambda hp, t, it, jt: (it[t], hp))
    ks = pl.BlockSpec((FOX_T, wide), lambda hp, t, it, jt: (jt[t], hp))
    whole = pl.BlockSpec((S, narrow), lambda hp, t, it, jt: (0, hp))
    cs = pl.BlockSpec((FOX_T, narrow), lambda hp, t, it, jt: (jt[t], hp))
    rs = pl.BlockSpec((FOX_HPS, 1, S), lambda hp, t, it, jt: (hp, 0, 0))
    shp = jax.ShapeDtypeStruct((S, FOX_WIDTH), _CD)
    grid_spec = pltpu.PrefetchScalarGridSpec(
        num_scalar_prefetch=2, grid=(N_FOX_HEADS // FOX_HPS, len(pairs)), in_specs=[qs, ks, ks, qs],
        out_specs=[whole, cs, cs, cs, rs],
        scratch_shapes=[pltpu.VMEM((S, wide), F32), pltpu.VMEM((FOX_T, wide), F32),
                        pltpu.VMEM((FOX_T, wide), F32), pltpu.VMEM((FOX_HPS, FOX_T, 1), F32),
                        pltpu.VMEM((FOX_HPS, 1, S), F32)])
    return _pcall(body, name=name, grid_spec=grid_spec,
                  out_shape=[shp, shp, shp, jax.ShapeDtypeStruct((S, FOX_WIDTH), F32),
                             jax.ShapeDtypeStruct((N_FOX_HEADS, 1, S), F32)],
                  compiler_params=_params("parallel", "arbitrary"))(i_tab, j_tab, qp, kp, vp, dop)


def _redilate(t, d):
    if d == 1:
        return t
    S, C = t.shape
    return t.reshape(d, S // d, C).transpose(1, 0, 2).reshape(S, C)


def _layer_step(x, tgt, w, p, late_weights=None, grad_sink=None, after=None, first_weights=None):
    S = x.shape[0]
    after_norm, after_proj = after if after is not None else (None, None)
    h = _rms_fwd(x, p["norm_mix_g"], name="rms_mix", after=after_norm)
    if first_weights is not None:
        w = {**w, **first_weights(h)}
    qkv = _mm(h, w["qkv"][:, 3 * DIL_WIDTH:], name="proj_fox", out_dtype=_CD, tn=768, tm=2048, after=after_proj)
    dil_qkv = _proj_dil(h, w["qkv"], name="proj_dil")
    zf = _mm(h, w["f"], name="proj_f")
    gl = _mm(h, w["g"], name="proj_gate", tn=1024)

    dil_o, dil_l = [], []
    for g in range(N_DIL_GROUPS):
        og, lg = _dil_fwd(dil_qkv[g], g, name=f"dil_fwd{g}")
        dil_o.append(og), dil_l.append(lg)
    o_a = _dil_mix_fwd(dil_o, dil_l, name="dil_mix")

    c = _fox_cumsum(zf, p["b_fgt"], name="fox_cumsum")
    fqp, fkp, fvp = _fox_pack_fwd(qkv, c, name="fox_pack")
    o_b, flse = _fox_fwd(fqp, fkp, fvp, name="fox_fwd")

    if late_weights is not None:
        w = {**w, **late_weights(o_b)}
    y_a = _mm(o_a, w["dil_out"], name="y_a", tn=1024, out_dtype=_CD)
    y_b = _mm(o_b, w["fox_out"], name="y_b", tn=1024, out_dtype=_CD)
    merged = _gate_fwd(gl, p["b_gate"], y_a, y_b, name="gate_fwd")
    x1 = _mm(merged, w["out"], name="mix_out", add=x)

    h2 = _rms_fwd(x1, p["norm_ffn_g"], name="rms_ffn")
    gu = _mm(h2, w["ffn_in"], name="ffn_in", tn=1408, b_blocks=True, out_dtype=_CD, tm=2048)
    act = _swiglu_fwd(gu, name="swiglu")
    x2 = _mm(act, w["ffn_down"], name="ffn_down", add=x1, tk=2816)

    loss, dx2, dg_final = _loss_head(x2, p["norm_final_g"], tgt, name="loss_head")

    dact = _mm(dx2, w["ffn_down"], name="d_act", tb=True, tn=1408, out_dtype=_CD)
    gw_ffn_down = _mm(act, dx2, name="gw_ffn_down", ta=True, out_dtype=_CD, tm=1408)
    dgu = _swiglu_bwd(dact, gu, name="swiglu_bwd")
    dh2 = _mm(dgu, w["ffn_in"], name="d_h2", tb=True, tk=1408, b_blocks=True, tm=2048)
    gw_ffn_in = _mm(h2, dgu, name="gw_ffn_in", ta=True, out_dtype=_CD, tn=1408, out_blocks=1408)
    sink = grad_sink if grad_sink is not None else (lambda group, grads: None)
    tok = sink("ffn", dict(ffn_in=gw_ffn_in, ffn_down=gw_ffn_down))
    dx1, dg_ffn = _rms_bwd(x1, p["norm_ffn_g"], dh2, dx2, name="rms_ffn_bwd", after=tok)

    dmerged = _mm(dx1, w["out"], name="d_merged", tb=True, out_dtype=_CD)
    gw_out = _mm(merged, dx1, name="gw_out", ta=True, out_dtype=_CD)
    dy_a, dy_b, dgl, db_gate = _gate_bwd(dmerged, gl, p["b_gate"], y_a, y_b, name="gate_bwd")
    do_a = _mm(dy_a, w["dil_out"], name="d_o_a", tb=True)
    gw_dil_out = _mm(o_a, dy_a, name="gw_dil_out", ta=True, out_dtype=_CD, tn=1024)
    do_b = _mm(dy_b, w["fox_out"], name="d_o_b", tb=True)
    gw_fox_out = _mm(o_b, dy_b, name="gw_fox_out", ta=True, out_dtype=_CD, tn=1024)
    tok = sink("mix", dict(dil_out=gw_dil_out, fox_out=gw_fox_out, out=gw_out))

    bqp, bdop = _fox_pack_bwd(qkv, c, o_b, flse, do_b, name="fox_pack_bwd", after=tok)
    dqp, dkp, dvp, dck, dcq = _fox_bwd(bqp, fkp, fvp, bdop, name="fox_bwd")
    dc = dcq[:, 0, :].T - dck.reshape(S, N_FOX_HEADS, HEAD_DIM)[:, :, 0]
    dc = jnp.pad(dc, ((0, 0), (0, F_PAD - N_FOX_HEADS)))
    dzf, db_fgt = _fox_cumsum_bwd(dc, zf, p["b_fgt"], name="fox_cumsum_bwd")

    douts = _dil_mix_bwd(do_a, dil_o, dil_l, name="dil_mix_bwd", after=tok)
    dqs, dks, dvs = [], [], []
    for g, (_, d) in enumerate(DIL_PAIRS):
        dq, dk, dv = _dil_bwd(dil_qkv[g], dil_o[g], dil_l[g], douts[g], douts[3 + g], g, name=f"dil_bwd{g}")
        dqs.append(_redilate(dq, d)), dks.append(_redilate(dk, d)), dvs.append(_redilate(dv, d))
    dqkv = jnp.concatenate(dqs + dks + dvs + [dqp, dkp, dvp], axis=1)

    gw_qkv = _mm(h, dqkv, name="gw_qkv", ta=True, out_dtype=_CD, tn=768)
    gw_g = _mm(h, dgl, name="gw_gate", ta=True, out_dtype=_CD)
    gw_f = _mm(h, dzf, name="gw_f", ta=True, out_dtype=_CD)
    tok = sink("in", dict(qkv=gw_qkv, f=gw_f, g=gw_g))
    dh = _mm(dqkv, w["qkv"], name="d_h_qkv", tb=True, tk=1920, tm=2048, after=tok)
    dh = _mm(dgl, w["g"], name="d_h_gate", tb=True, add=dh)
    dh = _mm(dzf, w["f"], name="d_h_f", tb=True, add=dh)
    dx, dg_mix = _rms_bwd(x, p["norm_mix_g"], dh, dx1, name="rms_mix_bwd")

    gw = dict(qkv=gw_qkv, f=gw_f, g=gw_g, dil_out=gw_dil_out, fox_out=gw_fox_out, out=gw_out, ffn_in=gw_ffn_in,
              ffn_down=gw_ffn_down)
    small = dict(norm_mix_g=dg_mix, b_fgt=db_fgt, b_gate=db_gate, norm_ffn_g=dg_ffn, norm_final_g=dg_final)
    return loss, dx, gw, small


def _position():
    return lax.axis_index("x"), lax.axis_index("y"), lax.axis_index("c")


def _other_chips(x, y):
    return [(1 - x, y), (x, 1 - y), (1 - x, 1 - y)]


ROW_TILE = 16


def _row_chunks(rows, want=4):
    n = want
    while n > 1 and rows % (n * ROW_TILE):
        n //= 2
    return n


SEM_SPEC = pl.BlockSpec(memory_space=pltpu.SEMAPHORE)
ANY_SPEC = pl.BlockSpec(memory_space=pl.ANY)
DATAFLOW = pltpu.SideEffectType.DATAFLOW_SIDE_EFFECTING


def _in_hbm(a):
    return pltpu.with_memory_space_constraint(a, pltpu.HBM)


def _split_copy_start(srcs, land_shapes, copies, after, *, name):
    n, m = len(srcs), len(land_shapes)

    def body(*refs):
        src_refs, land_refs = refs[:n], refs[n:n + m]
        send_sems, recv_sems = refs[n + m + 1], refs[n + m + 2]
        token = refs[-1]
        x, y, c = _position()
        for k, (src, dst, peer) in enumerate(copies(x, y, c, src_refs, land_refs)):
            pltpu.make_async_remote_copy(src_ref=src, dst_ref=dst, send_sem=send_sems.at[k], recv_sem=recv_sems.at[k],
                                         device_id=peer, device_id_type=MESH).start()
        token[...] = jnp.zeros_like(token)

    lands = [lax.empty(s.shape, s.dtype) for s in land_shapes]
    count = len(copies(0, 0, 0, srcs, lands))
    out = _pcall(
        body, name=name,
        out_shape=(pltpu.SemaphoreType.DMA((count,)), pltpu.SemaphoreType.DMA((count,)),
                   *[pltpu.HBM(s.shape, s.dtype) for s in srcs], *[pltpu.HBM(s.shape, s.dtype) for s in land_shapes],
                   jax.ShapeDtypeStruct((8, 128), F32)),
        in_specs=[HBM_SPEC] * (n + m) + [ANY_SPEC],
        out_specs=(SEM_SPEC, SEM_SPEC, *[HBM_SPEC] * (n + m), pl.BlockSpec(memory_space=pltpu.VMEM)),
        input_output_aliases={k: 2 + k for k in range(n + m)},
        compiler_params=pltpu.CompilerParams(has_side_effects=DATAFLOW),
    )(*[_in_hbm(s) for s in srcs], *[_in_hbm(l) for l in lands], after)
    return out[0], out[1], list(out[2:2 + n]), list(out[2 + n:2 + n + m]), out[-1]


def _split_copy_wait(send_sems, recv_sems, srcs, lands, copies, after, *, name):
    n, m = len(srcs), len(lands)

    def body(*refs):
        src_refs, land_refs = refs[:n], refs[n:n + m]
        send, recv = refs[n + m], refs[n + m + 1]
        x, y, c = _position()
        for k, (src, dst, peer) in enumerate(copies(x, y, c, src_refs, land_refs)):
            cp = pltpu.make_async_remote_copy(src_ref=src, dst_ref=dst, send_sem=send.at[k], recv_sem=recv.at[k],
                                              device_id=peer, device_id_type=MESH)
            cp.wait_send()
            cp.wait_recv()

    afters = list(after) if isinstance(after, (list, tuple)) else [after]
    out = _pcall(
        body, name=name,
        out_shape=tuple(pltpu.HBM(s.shape, s.dtype) for s in list(srcs) + list(lands)),
        in_specs=[HBM_SPEC] * (n + m) + [SEM_SPEC, SEM_SPEC] + [ANY_SPEC] * len(afters),
        out_specs=tuple([HBM_SPEC] * (n + m)),
        input_output_aliases={k: k for k in range(n + m)},
        compiler_params=pltpu.CompilerParams(has_side_effects=DATAFLOW),
    )(*srcs, *lands, send_sems, recv_sems, *afters)
    return list(out[:n]), list(out[n:])


def _gather_copies(x, y, c, shard_refs, land_refs):
    out = []
    for s, l in zip(shard_refs, land_refs):
        half = s.shape[0] // 2
        nq = _row_chunks(half)
        for cx, cy in _other_chips(x, y):
            for q in range(nq):
                rows = pl.ds(c * half + q * (half // nq), half // nq)
                out.append((s.at[rows, :], l.at[2 * x + y, rows, :], (cx, cy, c)))
    return out


def _scatter_copies(x, y, c, part_refs, land_refs):
    out = []
    for p, l in zip(part_refs, land_refs):
        nq = _row_chunks(p.shape[1])
        for r, (cx, cy) in enumerate(_other_chips(x, y)):
            for q in range(nq):
                rows = pl.ds(q * (p.shape[1] // nq), p.shape[1] // nq)
                out.append((p.at[2 * cx + cy, rows, :], l.at[r, rows, :], (cx, cy, c)))
    return out


def _forward_halves(lands, *, name):
    n = len(lands)

    def body(*refs):
        ins = refs[:n]
        send_sems, recv_sems = refs[2 * n:]
        x, y, c = _position()
        copies = []
        for w in range(n):
            half = ins[w].shape[1] // 2
            for r, (cx, cy) in enumerate(_other_chips(x, y)):
                blk = ins[w].at[2 * cx + cy, pl.ds(c * half, half), :]
                cp = pltpu.make_async_remote_copy(src_ref=blk, dst_ref=blk, send_sem=send_sems.at[w, r],
                                                  recv_sem=recv_sems.at[w, r], device_id=(x, y, 1 - c),
                                                  device_id_type=MESH)
                cp.start()
                copies.append(cp)
        for w in range(n):
            half = ins[w].shape[1] // 2
            for r, (cx, cy) in enumerate(_other_chips(x, y)):
                blk = ins[w].at[2 * cx + cy, pl.ds((1 - c) * half, half), :]
                pltpu.make_async_remote_copy(src_ref=blk, dst_ref=blk, send_sem=send_sems.at[w, r],
                                             recv_sem=recv_sems.at[w, r], device_id=(x, y, 1 - c),
                                             device_id_type=MESH).wait_recv()
        for cp in copies:
            cp.wait_send()

    return _pcall(
        body, name=name, in_specs=[HBM_SPEC] * n, out_specs=[HBM_SPEC] * n,
        out_shape=[jax.ShapeDtypeStruct(l.shape, l.dtype) for l in lands],
        input_output_aliases={k: k for k in range(n)},
        scratch_shapes=[pltpu.SemaphoreType.DMA((n, 3)), pltpu.SemaphoreType.DMA((n, 3))],
    )(*lands)


def _swap_halves(grads, name="swap_halves"):
    n = len(grads)

    def body(*refs):
        ins, outs = refs[:n], refs[n:2 * n]
        send_sems, recv_sems = refs[2 * n:]
        x, y, c = _position()
        copies = []
        for w in range(n):
            half = ins[w].shape[1] // 2
            cp = pltpu.make_async_remote_copy(
                src_ref=ins[w].at[:, pl.ds((1 - c) * half, half), :], dst_ref=outs[w], send_sem=send_sems.at[w],
                recv_sem=recv_sems.at[w], device_id=(x, y, 1 - c), device_id_type=MESH)
            cp.start()
            copies.append(cp)
        for cp in copies:
            cp.wait()

    return _pcall(
        body, name=name, in_specs=[HBM_SPEC] * n, out_specs=[HBM_SPEC] * n,
        out_shape=[jax.ShapeDtypeStruct((4, g.shape[1] // 2, g.shape[2]), g.dtype) for g in grads],
        scratch_shapes=[pltpu.SemaphoreType.DMA((n,)), pltpu.SemaphoreType.DMA((n,))],
    )(*grads)


def _share_halves(halves):
    n = len(halves)

    def body(*refs):
        ins, outs = refs[:n], refs[n:2 * n]
        send_sems, recv_sems = refs[2 * n:]
        x, y, c = _position()
        copies = []
        for w in range(n):
            cp = pltpu.make_async_remote_copy(src_ref=ins[w], dst_ref=outs[w], send_sem=send_sems.at[w],
                                              recv_sem=recv_sems.at[w], device_id=(x, y, 1 - c), device_id_type=MESH)
            cp.start()
            copies.append(cp)
        for cp in copies:
            cp.wait()

    return _pcall(
        body, name="share_halves", in_specs=[HBM_SPEC] * n, out_specs=[HBM_SPEC] * n,
        out_shape=[jax.ShapeDtypeStruct(h.shape, h.dtype) for h in halves],
        scratch_shapes=[pltpu.SemaphoreType.DMA((n,)), pltpu.SemaphoreType.DMA((n,))],
    )(*halves)


def _sum_small(part):
    rows, width = part.shape

    def body(x_ref, out_ref, all_ref, send_sems, recv_sems):
        x, y, c = _position()
        me, sibling = (x, y, c), (x, y, 1 - c)
        chips = _other_chips(x, y)

        def block(px, py, pc):
            return all_ref.at[pl.ds((4 * px + 2 * py + pc) * rows, rows), :]

        def copy(k, blk, to, src=None):
            return pltpu.make_async_remote_copy(
                src_ref=block(*blk) if src is None else src, dst_ref=block(*blk), send_sem=send_sems.at[k],
                recv_sem=recv_sems.at[k], device_id=to, device_id_type=MESH)

        all_ref[pl.ds((4 * x + 2 * y + c) * rows, rows), :] = x_ref[...]
        first = [copy(0, me, sibling, src=x_ref)]
        first += [copy(1 + j, me, (*chip, c), src=x_ref) for j, chip in enumerate(chips)]
        for cp in first:
            cp.start()
        passed = [copy(4 + j, (*chip, c), sibling) for j, chip in enumerate(chips)]
        for j, chip in enumerate(chips):
            copy(1 + j, (*chip, c), me).wait_recv()
            passed[j].start()
        copy(0, sibling, me).wait_recv()
        for j, chip in enumerate(chips):
            copy(4 + j, (*chip, 1 - c), me).wait_recv()
        for cp in first + passed:
            cp.wait_send()
        total = all_ref[0:rows, :]
        for d in range(1, 8):
            total = total + all_ref[d * rows:(d + 1) * rows, :]
        out_ref[...] = total

    vm = pl.BlockSpec(memory_space=pltpu.VMEM)
    return _pcall(
        body, name="sum_small", in_specs=[vm], out_specs=vm, out_shape=jax.ShapeDtypeStruct((rows, width), F32),
        scratch_shapes=[pltpu.VMEM((8 * rows, width), F32), pltpu.SemaphoreType.DMA((7,)), pltpu.SemaphoreType.DMA((7,))],
    )(part)


def _row_tile(R, C, itemsize=4, budget=1 << 20):
    for t in (512, 256, 128, 64, 32, 16, 8):
        if R % t == 0 and t * C * itemsize <= budget:
            return t
    return R


def _add_halves(g, recv, c, *, name):
    _, R, C = g.shape
    half = R // 2
    t = _row_tile(half, C)
    nb = half // t

    def body(c_ref, g_ref, r_ref, o_ref):
        o_ref[...] = (g_ref[...].astype(F32) + r_ref[...].astype(F32)).astype(o_ref.dtype)

    grid_spec = pltpu.PrefetchScalarGridSpec(
        num_scalar_prefetch=1, grid=(4, nb),
        in_specs=[pl.BlockSpec((1, t, C), lambda k, i, cr: (k, cr[0] * nb + i, 0)),
                  pl.BlockSpec((1, t, C), lambda k, i, cr: (k, i, 0))],
        out_specs=pl.BlockSpec((1, t, C), lambda k, i, cr: (k, i, 0)))
    return _pcall(body, name=name, grid_spec=grid_spec, out_shape=jax.ShapeDtypeStruct((4, half, C), g.dtype),
                  compiler_params=_params("parallel", "parallel"))(c, g, recv)


def _add_owners(mine, recv, *, name):
    half, C = mine.shape
    t = _row_tile(half, C)

    def body(m_ref, r_ref, o_ref):
        o_ref[...] = ((m_ref[...].astype(F32) + r_ref[0].astype(F32)) + r_ref[1].astype(F32)) + r_ref[2].astype(F32)

    return _pcall(body, name=name, grid=(half // t,),
                  in_specs=[pl.BlockSpec((t, C), lambda i: (i, 0)), pl.BlockSpec((3, t, C), lambda i: (0, i, 0))],
                  out_specs=pl.BlockSpec((t, C), lambda i: (i, 0)), out_shape=jax.ShapeDtypeStruct((half, C), F32),
                  compiler_params=_params("parallel"))(mine, recv)


def _adamw(w, g, m, v, *, name):
    R, C = w.shape
    t = _row_tile(R, C)
    c1 = 1.0 - ADAM_B1 ** ADAM_STEP
    c2 = 1.0 - ADAM_B2 ** ADAM_STEP

    def body(w_ref, g_ref, m_ref, v_ref, d_ref, nm_ref, nv_ref):
        gv = g_ref[...]
        mn = ADAM_B1 * m_ref[...] + (1.0 - ADAM_B1) * gv
        vn = ADAM_B2 * v_ref[...] + (1.0 - ADAM_B2) * (gv * gv)
        d_ref[...] = -ADAM_LR * ((mn / c1) / (jnp.sqrt(vn / c2) + ADAM_EPS) + ADAM_WD * w_ref[...])
        nm_ref[...] = mn
        nv_ref[...] = vn

    blk = pl.BlockSpec((t, C), lambda i: (i, 0))
    shp = jax.ShapeDtypeStruct((R, C), F32)
    return _pcall(body, name=name, grid=(R // t,), in_specs=[blk] * 4, out_specs=[blk] * 3, out_shape=[shp] * 3,
                  compiler_params=_params("parallel"))(w, g, m, v)


BIG = ("w_in", "w_dil_out", "w_fox_out", "w_out", "w_ffn_in", "w_ffn_down")
SMALL = ("norm_mix_g", "b_fgt", "b_gate", "norm_ffn_g", "norm_final_g")
ORDER = ("norm_mix_g", "w_in", "b_fgt", "b_gate", "w_dil_out", "w_fox_out", "w_out", "norm_ffn_g", "w_ffn_in",
         "w_ffn_down", "norm_final_g")
SMALL_ROWS = {"norm_mix_g": (0, 1), "b_gate": (1, 3), "norm_ffn_g": (3, 4), "norm_final_g": (4, 5), "b_fgt": (5, 6)}


def _columns_to_blocks(full, ncol):
    K = full.shape[0]
    return full.reshape(K, 4, ncol).transpose(1, 0, 2)


def _blocks_to_columns(blocks):
    n, K, ncol = blocks.shape
    return blocks.transpose(1, 0, 2).reshape(K, n * ncol)


def kernel(x, norm_mix_g, w_in, b_fgt, b_gate, w_dil_out, w_fox_out, w_out, norm_ffn_g, w_ffn_in, w_ffn_down, norm_final_g, loss_target, m_norm_mix_g, m_w_in, m_b_fgt, m_b_gate, m_w_dil_out, m_w_fox_out, m_w_out, m_norm_ffn_g, m_w_ffn_in, m_w_ffn_down, m_norm_final_g, v_norm_mix_g, v_w_in, v_b_fgt, v_b_gate, v_w_dil_out, v_w_fox_out, v_w_out, v_norm_ffn_g, v_w_ffn_in, v_w_ffn_down, v_norm_final_g):
    weights = dict(norm_mix_g=norm_mix_g, w_in=w_in, b_fgt=b_fgt, b_gate=b_gate, w_dil_out=w_dil_out,
                   w_fox_out=w_fox_out, w_out=w_out, norm_ffn_g=norm_ffn_g, w_ffn_in=w_ffn_in, w_ffn_down=w_ffn_down,
                   norm_final_g=norm_final_g)
    m_in = dict(norm_mix_g=m_norm_mix_g, w_in=m_w_in, b_fgt=m_b_fgt, b_gate=m_b_gate, w_dil_out=m_w_dil_out,
                w_fox_out=m_w_fox_out, w_out=m_w_out, norm_ffn_g=m_norm_ffn_g, w_ffn_in=m_w_ffn_in,
                w_ffn_down=m_w_ffn_down, norm_final_g=m_norm_final_g)
    v_in = dict(norm_mix_g=v_norm_mix_g, w_in=v_w_in, b_fgt=v_b_fgt, b_gate=v_b_gate, w_dil_out=v_w_dil_out,
                w_fox_out=v_w_fox_out, w_out=v_w_out, norm_ffn_g=v_norm_ffn_g, w_ffn_in=v_w_ffn_in,
                w_ffn_down=v_w_ffn_down, norm_final_g=v_norm_final_g)
    c = lax.axis_index("c")
    chip = 2 * lax.axis_index("x") + lax.axis_index("y")

    shards = {n: weights[n][0].astype(_CD) for n in BIG}
    in_shape = jax.ShapeDtypeStruct((4,) + shards["w_in"].shape, _CD)
    send_i, recv_i, in_src, in_land, token_in = _split_copy_start(
        [shards["w_in"]], [in_shape], _gather_copies, norm_mix_g, name="gather_in_start")
    late = BIG[1:]
    send_g, recv_g, late_src, late_land, token = _split_copy_start(
        [shards[n] for n in late], [jax.ShapeDtypeStruct((4,) + shards[n].shape, _CD) for n in late],
        _gather_copies, token_in, name="gather_late_start")
    adam_in = [t[0] + token_in[0, 0] for t in (w_in, m_w_in, v_w_in)]
    p = dict(norm_mix_g=norm_mix_g, b_fgt=jnp.pad(b_fgt, ((0, 0), (0, F_PAD - N_FOX_HEADS))), b_gate=b_gate,
             norm_ffn_g=norm_ffn_g, norm_final_g=norm_final_g.reshape(1, D_MODEL))

    def first_weights(after):
        own, lands = _split_copy_wait(send_i, recv_i, in_src, in_land, _gather_copies, [after] + adam_in,
                                      name="gather_in_wait")
        (g_in,) = _forward_halves(lands, name="gather_in_forward")
        full_in = _blocks_to_columns(lax.dynamic_update_index_in_dim(g_in, own[0], chip, 0))
        o3 = QKV_COLS
        o4 = o3 + N_FOX_HEADS
        return dict(qkv=full_in[:, :o3], f=jnp.pad(full_in[:, o3:o4], ((0, 0), (0, F_PAD - N_FOX_HEADS))),
                    g=full_in[:, o4:])

    def late_weights(after):
        own, lands = _split_copy_wait(send_g, recv_g, late_src, late_land, _gather_copies, after,
                                      name="gather_late_wait")
        lands = _forward_halves(lands, name="gather_late_forward")
        g_dil, g_fox, g_out, g_ffn_in, g_ffn_down = [
            lax.dynamic_update_index_in_dim(l, s, chip, 0) for l, s in zip(lands, own)]
        return dict(dil_out=_blocks_to_columns(g_dil), fox_out=_blocks_to_columns(g_fox),
                    out=g_out.reshape(D_MODEL, D_MODEL), ffn_in=g_ffn_in,
                    ffn_down=g_ffn_down.reshape(D_FF, D_MODEL))

    c_arr = jnp.reshape(c, (1,)).astype(jnp.int32)

    def to_blocks(n, full):
        shape = weights[n].shape
        if full.ndim == 3:
            return full
        if n in ("w_out", "w_ffn_down"):
            return full.reshape(4, shape[1], shape[2])
        return _columns_to_blocks(full, shape[2])

    def pair_sums(group, named):
        names = list(named)
        blocks = [to_blocks(n, named[n]) for n in names]
        from_sibling = _swap_halves(blocks, name=f"swap_halves_{group}")
        return [_add_halves(b, r, c_arr, name=f"add_halves_{n}") for b, r, n in zip(blocks, from_sibling, names)]

    in_flight = {}

    def grad_sink(group, gw):
        if group == "in":
            named = {"w_in": jnp.concatenate([gw["qkv"], gw["f"][:, :N_FOX_HEADS], gw["g"]], axis=1)}
        else:
            named = {"w_" + k: v for k, v in gw.items()}
        sums = pair_sums(group, named)
        started = _split_copy_start(sums, [jax.ShapeDtypeStruct((3,) + s.shape[1:], s.dtype) for s in sums],
                                    _scatter_copies, next(iter(gw.values())), name=f"scatter_{group}_start")
        in_flight[group] = (list(named), started)
        return started[-1]

    loss_part, grad_x, gw, small = _layer_step(x[0], loss_target[0], {}, p, late_weights, grad_sink,
                                               (token_in, token), first_weights)

    def owner_sums(names, sums, from_chips):
        return {n: _add_owners(lax.dynamic_index_in_dim(s, chip, 0, keepdims=False), r, name=f"add_owners_{n}")
                for n, s, r in zip(names, sums, from_chips)}

    halves = {}
    for group, (names, (send_s, recv_s, srcs, lands, _)) in in_flight.items():
        sums, from_chips = _split_copy_wait(send_s, recv_s, srcs, lands, _scatter_copies, grad_x,
                                            name=f"scatter_{group}_wait")
        halves.update(owner_sums(names, sums, from_chips))
    halves = [halves[n] for n in BIG]
    grads = {}
    for n, own, other in zip(BIG, halves, _share_halves(halves)):
        pair = jnp.stack([own, other])
        grads[n] = jnp.where(c == 0, pair, pair[::-1]).reshape(2 * own.shape[0], own.shape[1])

    packed = jnp.concatenate([
        small["norm_mix_g"], small["b_gate"].reshape(2, D_MODEL), small["norm_ffn_g"], small["norm_final_g"],
        jnp.pad(small["b_fgt"], ((0, 0), (0, D_MODEL - F_PAD))), jnp.zeros((2, D_MODEL), F32)], axis=0)
    summed = _sum_small(packed)
    for n in SMALL:
        lo, hi = SMALL_ROWS[n]
        grads[n] = summed[lo:hi].reshape(1, -1)[:, :weights[n].size]

    loss = lax.psum(loss_part[0, 0], ("x", "y", "c"))

    out_g, out_d, out_m, out_v = {}, {}, {}, {}
    for n in ORDER:
        shape = weights[n].shape
        two_d = shape[1:] if len(shape) == 3 else (1, weights[n].size)
        g2 = grads[n].reshape(two_d)
        wmv = adam_in if n == "w_in" else [t.reshape(two_d) for t in (weights[n], m_in[n], v_in[n])]
        d2, m2, v2 = _adamw(wmv[0], g2, wmv[1], wmv[2], name=f"adamw_{n}")
        out_g[n], out_d[n], out_m[n], out_v[n] = (g2.reshape(shape), d2.reshape(shape), m2.reshape(shape),
                                                  v2.reshape(shape))
    return (loss, grad_x[None], *[out_g[n] for n in ORDER], *[out_d[n] for n in ORDER],
            *[out_m[n] for n in ORDER], *[out_v[n] for n in ORDER])
```

```python
import numpy as np
import jax
import jax.numpy as jnp
from jax import lax
from jax.experimental import pallas as pl
from jax.experimental.pallas import tpu as pltpu

F32 = jnp.float32
_CD = jnp.bfloat16

D_MODEL = 1024
HEAD_DIM = 64
DIL_PAIRS = ((128, 1), (512, 4), (2048, 16))
N_DIL_GROUPS = 3
DIL_HEADS = 4
DIL_W = 128
DIL_OUT = DIL_HEADS * HEAD_DIM
DIL_WIDTH = N_DIL_GROUPS * DIL_OUT
N_FOX_HEADS = 8
FOX_WIDTH = N_FOX_HEADS * HEAD_DIM
D_FF = 2816
QKV_COLS = 3 * DIL_WIDTH + 3 * FOX_WIDTH
F_PAD = 128
RMS_EPS = 1e-6
NEG_INF = -1e30
ATTN_SCALE = HEAD_DIM ** -0.5
ADAM_LR, ADAM_B1, ADAM_B2, ADAM_EPS, ADAM_WD, ADAM_STEP = 0.001, 0.9, 0.999, 1e-08, 0.01, 10

VMEM_LIMIT = 48 * 1024 * 1024
LANES = 128
MESH = pl.DeviceIdType.MESH
HBM_SPEC = pl.BlockSpec(memory_space=pltpu.HBM)


def _pcall(body, after=None, **kw):
    if after is None:
        return pl.pallas_call(body, **kw)
    n_in = len(kw["in_specs"])
    kw["in_specs"] = list(kw["in_specs"]) + [pl.BlockSpec(memory_space=pl.ANY)]

    def tied(*refs):
        return body(*refs[:n_in], *refs[n_in + 1:])

    call = pl.pallas_call(tied, **kw)
    return lambda *args: call(*args, after)


def _params(*sem):
    return pltpu.CompilerParams(dimension_semantics=sem, vmem_limit_bytes=VMEM_LIMIT)


def _pick(dim, pref):
    t = (min(pref, dim) // 128) * 128
    while t >= 128:
        if dim % t == 0:
            return t
        t -= 128
    return dim


def _mm(a, b, *, name, ta=False, tb=False, out_dtype=F32, add=None, tm=1024, tn=512, tk=2048, after=None,
        b_blocks=False, out_blocks=None):
    if ta:
        K, M = a.shape
    else:
        M, K = a.shape
    b_rows, b_cols = (b.shape[1], b.shape[0] * b.shape[2]) if b_blocks else b.shape
    if tb:
        N, K2 = b_rows, b_cols
    else:
        K2, N = b_rows, b_cols
    assert K == K2, (a.shape, b.shape)
    shard = b.shape[2] if b_blocks else None
    tm = _pick(M, tm)
    tn = _pick(shard if (b_blocks and not tb) else (out_blocks or N), tn)
    tk = _pick(shard if (b_blocks and tb) else K, tk)
    nk = K // tk
    dn = (((0 if ta else 1,), (1 if tb else 0,)), ((), ()))
    has_add = add is not None
    assert not (has_add and out_blocks)

    def body(*refs):
        a_ref, b_ref = refs[0], refs[1]
        add_ref = refs[2] if has_add else None
        o_ref = refs[3] if has_add else refs[2]
        bv = b_ref[0] if b_blocks else b_ref[...]
        p = lax.dot_general(a_ref[...].astype(_CD), bv.astype(_CD), dn, preferred_element_type=F32)

        def finish(r):
            if has_add:
                r = r + add_ref[...]
            if out_blocks:
                o_ref[0] = r.astype(out_dtype)
            else:
                o_ref[...] = r.astype(out_dtype)

        if nk == 1:
            finish(p)
        else:
            acc_ref = refs[-1]
            k = pl.program_id(2)

            @pl.when(k == 0)
            def _():
                acc_ref[...] = p

            @pl.when(k > 0)
            def _():
                acc_ref[...] += p

            @pl.when(k == nk - 1)
            def _():
                finish(acc_ref[...])

    a_spec = pl.BlockSpec((tk, tm), lambda i, j, k: (k, i)) if ta else pl.BlockSpec((tm, tk), lambda i, j, k: (i, k))
    if b_blocks and tb:
        per = shard // tk
        b_spec = pl.BlockSpec((1, tn, tk), lambda i, j, k: (k // per, j, k % per))
    elif b_blocks:
        per = shard // tn
        b_spec = pl.BlockSpec((1, tk, tn), lambda i, j, k: (j // per, k, j % per))
    else:
        b_spec = pl.BlockSpec((tn, tk), lambda i, j, k: (j, k)) if tb else pl.BlockSpec((tk, tn), lambda i, j, k: (k, j))
    if out_blocks:
        oper = out_blocks // tn
        o_spec = pl.BlockSpec((1, tm, tn), lambda i, j, k: (j // oper, i, j % oper))
        out_shape = jax.ShapeDtypeStruct((N // out_blocks, M, out_blocks), out_dtype)
    else:
        o_spec = pl.BlockSpec((tm, tn), lambda i, j, k: (i, j))
        out_shape = jax.ShapeDtypeStruct((M, N), out_dtype)
    in_specs = [a_spec, b_spec] + ([o_spec] if has_add else [])
    args = (a, b) + ((add,) if has_add else ())
    return _pcall(
        body, after, name=name, grid=(M // tm, N // tn, nk), in_specs=in_specs, out_specs=o_spec,
        out_shape=out_shape,
        scratch_shapes=[pltpu.VMEM((tm, tn), F32)] if nk > 1 else [],
        compiler_params=_params("parallel", "parallel", "arbitrary"),
    )(*args)


def _rms_fwd(x, g, *, name, tm=512, after=None):
    S, D = x.shape

    def body(x_ref, g_ref, h_ref):
        xv = x_ref[...]
        r = lax.rsqrt(jnp.mean(xv * xv, axis=-1, keepdims=True) + RMS_EPS)
        h_ref[...] = ((xv * r) * g_ref[...]).astype(h_ref.dtype)

    row = pl.BlockSpec((tm, D), lambda i: (i, 0))
    return _pcall(body, after, name=name, grid=(S // tm,), in_specs=[row, pl.BlockSpec((1, D), lambda i: (0, 0))],
                  out_specs=row, out_shape=jax.ShapeDtypeStruct((S, D), _CD), compiler_params=_params("parallel"))(x, g)


def _rms_bwd(x, g, dh, dres, *, name, tm=512, after=None):
    S, D = x.shape

    def body(x_ref, g_ref, dh_ref, dres_ref, dx_ref, dg_ref):
        xv = x_ref[...]
        r = lax.rsqrt(jnp.mean(xv * xv, axis=-1, keepdims=True) + RMS_EPS)
        xh = xv * r
        dhv = dh_ref[...]
        dxh = dhv * g_ref[...]
        dx_ref[...] = dres_ref[...] + r * (dxh - xh * jnp.mean(dxh * xh, axis=-1, keepdims=True))
        part = jnp.sum(dhv * xh, axis=0, keepdims=True)

        @pl.when(pl.program_id(0) == 0)
        def _():
            dg_ref[...] = part

        @pl.when(pl.program_id(0) > 0)
        def _():
            dg_ref[...] += part

    row = pl.BlockSpec((tm, D), lambda i: (i, 0))
    vec = pl.BlockSpec((1, D), lambda i: (0, 0))
    return _pcall(body, after, name=name, grid=(S // tm,), in_specs=[row, vec, row, row], out_specs=[row, vec],
                  out_shape=[jax.ShapeDtypeStruct((S, D), F32), jax.ShapeDtypeStruct((1, D), F32)],
                  compiler_params=_params("arbitrary"))(x, g, dh, dres)


def _loss_head(x, g, tgt, *, name, tm=512):
    S, D = x.shape

    def body(x_ref, g_ref, t_ref, loss_ref, dx_ref, dg_ref):
        xv = x_ref[...]
        gv = g_ref[...]
        r = lax.rsqrt(jnp.mean(xv * xv, axis=-1, keepdims=True) + RMS_EPS)
        xh = xv * r
        err = xh * gv - t_ref[...]
        lpart = 0.5 * jnp.sum(jnp.mean(err * err, axis=-1, keepdims=True), axis=0, keepdims=True)
        dy = err * (1.0 / D)
        dxh = dy * gv
        dx_ref[...] = r * (dxh - xh * jnp.mean(dxh * xh, axis=-1, keepdims=True))
        gpart = jnp.sum(dy * xh, axis=0, keepdims=True)

        @pl.when(pl.program_id(0) == 0)
        def _():
            loss_ref[...] = lpart
            dg_ref[...] = gpart

        @pl.when(pl.program_id(0) > 0)
        def _():
            loss_ref[...] += lpart
            dg_ref[...] += gpart

    row = pl.BlockSpec((tm, D), lambda i: (i, 0))
    vec = pl.BlockSpec((1, D), lambda i: (0, 0))
    one = pl.BlockSpec((1, 1), lambda i: (0, 0))
    return _pcall(body, name=name, grid=(S // tm,), in_specs=[row, vec, row], out_specs=[one, row, vec],
                  out_shape=[jax.ShapeDtypeStruct((1, 1), F32), jax.ShapeDtypeStruct((S, D), F32),
                             jax.ShapeDtypeStruct((1, D), F32)],
                  compiler_params=_params("arbitrary"))(x, g, tgt)


def _sigmoid(z):
    return 1.0 / (1.0 + jnp.exp(-z))


def _gate_fwd(gl, bg, ya, yb, *, name, tm=512):
    S, D = ya.shape

    def body(za_ref, zb_ref, ba_ref, bb_ref, ya_ref, yb_ref, o_ref):
        ga = _sigmoid(za_ref[...].astype(F32) + ba_ref[...])
        gb = _sigmoid(zb_ref[...].astype(F32) + bb_ref[...])
        o_ref[...] = (ga * ya_ref[...].astype(F32) + gb * yb_ref[...].astype(F32)).astype(o_ref.dtype)

    lo = pl.BlockSpec((tm, D), lambda i: (i, 0))
    hi = pl.BlockSpec((tm, D), lambda i: (i, 1))
    vlo = pl.BlockSpec((1, D), lambda i: (0, 0))
    vhi = pl.BlockSpec((1, D), lambda i: (0, 1))
    return _pcall(body, name=name, grid=(S // tm,), in_specs=[lo, hi, vlo, vhi, lo, lo], out_specs=lo,
                  out_shape=jax.ShapeDtypeStruct((S, D), _CD), compiler_params=_params("parallel"))(gl, gl, bg, bg, ya, yb)


def _gate_bwd(dm, gl, bg, ya, yb, *, name, tm=256):
    S, D = ya.shape

    def body(dm_ref, za_ref, zb_ref, ba_ref, bb_ref, ya_ref, yb_ref, dya_ref, dyb_ref, dgl_ref, dbg_ref):
        dmv = dm_ref[...].astype(F32)
        ga = _sigmoid(za_ref[...].astype(F32) + ba_ref[...])
        gb = _sigmoid(zb_ref[...].astype(F32) + bb_ref[...])
        dya_ref[...] = (dmv * ga).astype(dya_ref.dtype)
        dyb_ref[...] = (dmv * gb).astype(dyb_ref.dtype)
        dza = dmv * ya_ref[...].astype(F32) * ga * (1.0 - ga)
        dzb = dmv * yb_ref[...].astype(F32) * gb * (1.0 - gb)
        dgl_ref[:, :D] = dza.astype(dgl_ref.dtype)
        dgl_ref[:, D:] = dzb.astype(dgl_ref.dtype)
        pa = jnp.sum(dza, axis=0, keepdims=True)
        pb = jnp.sum(dzb, axis=0, keepdims=True)

        @pl.when(pl.program_id(0) == 0)
        def _():
            dbg_ref[:, :D] = pa
            dbg_ref[:, D:] = pb

        @pl.when(pl.program_id(0) > 0)
        def _():
            dbg_ref[:, :D] += pa
            dbg_ref[:, D:] += pb

    lo = pl.BlockSpec((tm, D), lambda i: (i, 0))
    hi = pl.BlockSpec((tm, D), lambda i: (i, 1))
    vlo = pl.BlockSpec((1, D), lambda i: (0, 0))
    vhi = pl.BlockSpec((1, D), lambda i: (0, 1))
    wide = pl.BlockSpec((tm, 2 * D), lambda i: (i, 0))
    vwide = pl.BlockSpec((1, 2 * D), lambda i: (0, 0))
    return _pcall(body, name=name, grid=(S // tm,), in_specs=[lo, lo, hi, vlo, vhi, lo, lo],
                  out_specs=[lo, lo, wide, vwide],
                  out_shape=[jax.ShapeDtypeStruct((S, D), _CD), jax.ShapeDtypeStruct((S, D), _CD),
                             jax.ShapeDtypeStruct((S, 2 * D), _CD), jax.ShapeDtypeStruct((1, 2 * D), F32)],
                  compiler_params=_params("arbitrary"))(dm, gl, gl, bg, bg, ya, yb)


def _ffn_in_act(h2, w_blocks, *, name, tm=512):
    S, D = h2.shape
    _, _, C = w_blocks.shape

    def body(a_ref, bg_ref, bu_ref, g_ref, u_ref, o_ref):
        av = a_ref[...].astype(_CD)
        gv = jnp.dot(av, bg_ref[0].astype(_CD), preferred_element_type=F32)
        uv = jnp.dot(av, bu_ref[0].astype(_CD), preferred_element_type=F32)
        g_ref[...] = gv.astype(g_ref.dtype)
        u_ref[...] = uv.astype(u_ref.dtype)
        o_ref[...] = (gv * _sigmoid(gv) * uv).astype(o_ref.dtype)

    out = pl.BlockSpec((tm, C), lambda i, j: (i, j))
    shp = jax.ShapeDtypeStruct((S, 2 * C), _CD)
    return _pcall(body, name=name, grid=(S // tm, 2),
                  in_specs=[pl.BlockSpec((tm, D), lambda i, j: (i, 0)), pl.BlockSpec((1, D, C), lambda i, j: (j, 0, 0)),
                            pl.BlockSpec((1, D, C), lambda i, j: (2 + j, 0, 0))],
                  out_specs=[out, out, out], out_shape=[shp, shp, shp],
                  compiler_params=_params("parallel", "arbitrary"))(h2, w_blocks, w_blocks)


def _swiglu_bwd(dact, gate, up, *, name, tm=256):
    S, F = gate.shape
    F2 = 2 * F

    def body(d_ref, g_ref, u_ref, o_ref):
        dv = d_ref[...].astype(F32)
        gv = g_ref[...].astype(F32)
        sg = _sigmoid(gv)
        o_ref[:, :F] = (dv * u_ref[...].astype(F32) * (sg * (1.0 + gv * (1.0 - sg)))).astype(o_ref.dtype)
        o_ref[:, F:] = (dv * (gv * sg)).astype(o_ref.dtype)

    lo = pl.BlockSpec((tm, F), lambda i: (i, 0))
    return _pcall(body, name=name, grid=(S // tm,), in_specs=[lo, lo, lo],
                  out_specs=pl.BlockSpec((tm, F2), lambda i: (i, 0)),
                  out_shape=jax.ShapeDtypeStruct((S, F2), _CD), compiler_params=_params("parallel"))(dact, gate, up)


def _split3(x):
    hi = x.astype(jnp.bfloat16)
    r1 = x - hi.astype(F32)
    mid = r1.astype(jnp.bfloat16)
    lo = (r1 - mid.astype(F32)).astype(jnp.bfloat16)
    return hi, mid, lo


def _ones_dot_left(ones, x):
    return sum(jnp.dot(ones, p, preferred_element_type=F32) for p in _split3(x))


def _ones_dot_right(x, ones):
    return sum(jnp.dot(p, ones, preferred_element_type=F32) for p in _split3(x))


def _head_sum(x):
    n = x.shape[1]
    r = lax.broadcasted_iota(jnp.int32, (n, n), 0) // HEAD_DIM
    c = lax.broadcasted_iota(jnp.int32, (n, n), 1) // HEAD_DIM
    return _ones_dot_right(x, (r == c).astype(jnp.bfloat16))


def _log_sigmoid(z):
    e = jnp.exp(-jnp.abs(z))
    t = 1.0 + e
    log1p_e = jnp.where(t == 1.0, e, jnp.log(t) * (e / jnp.where(t == 1.0, 1.0, t - 1.0)))
    return jnp.minimum(z, 0.0) - log1p_e


def _fox_cumsum(zf, bf, *, name):
    S, W = zf.shape
    nb = S // 128

    def body(z_ref, b_ref, c_ref):
        tri = (lax.broadcasted_iota(jnp.int32, (128, 128), 0) >= lax.broadcasted_iota(jnp.int32, (128, 128), 1))
        tri = tri.astype(jnp.bfloat16)

        def step(i, carry):
            rows = pl.ds(pl.multiple_of(i * 128, 128), 128)
            lf = _log_sigmoid(z_ref[rows, :] + b_ref[...])
            cb = _ones_dot_left(tri, lf) + carry
            c_ref[rows, :] = cb
            return cb[127:128, :]

        lax.fori_loop(0, nb, step, jnp.zeros((1, W), F32))

    return _pcall(body, name=name, out_shape=jax.ShapeDtypeStruct((S, W), F32),
                  compiler_params=pltpu.CompilerParams(vmem_limit_bytes=VMEM_LIMIT))(zf, bf)


def _fox_cumsum_bwd(dc, zf, bf, *, name):
    S, W = zf.shape
    nb = S // 128

    def body(dc_ref, z_ref, b_ref, dz_ref, db_ref):
        tri = (lax.broadcasted_iota(jnp.int32, (128, 128), 0) <= lax.broadcasted_iota(jnp.int32, (128, 128), 1))
        tri = tri.astype(jnp.bfloat16)

        def step(k, carry):
            tail, acc = carry
            i = nb - 1 - k
            rows = pl.ds(pl.multiple_of(i * 128, 128), 128)
            dlf = _ones_dot_left(tri, dc_ref[rows, :]) + tail
            dz = dlf * _sigmoid(-(z_ref[rows, :] + b_ref[...]))
            dz_ref[rows, :] = dz
            return dlf[0:1, :], acc + jnp.sum(dz, axis=0, keepdims=True)

        _, acc = lax.fori_loop(0, nb, step, (jnp.zeros((1, W), F32), jnp.zeros((1, W), F32)))
        db_ref[...] = acc

    return _pcall(body, name=name,
                  out_shape=[jax.ShapeDtypeStruct((S, W), F32), jax.ShapeDtypeStruct((1, W), F32)],
                  compiler_params=pltpu.CompilerParams(vmem_limit_bytes=VMEM_LIMIT))(dc, zf, bf)


def _proj_dil(h, w_qkv, *, name, tm=1024):
    S, D = h.shape
    tn = DIL_WIDTH

    def body(a_ref, b_ref, *rest):
        outs, acc = rest[:N_DIL_GROUPS], rest[N_DIL_GROUPS]
        prod = jnp.dot(a_ref[...].astype(_CD), b_ref[...].astype(_CD), preferred_element_type=F32)
        for k in range(tn // LANES):
            acc[k] = prod[:, k * LANES:(k + 1) * LANES]
        for g, (_, d) in enumerate(DIL_PAIRS):
            for half in range(DIL_OUT // LANES):
                k = g * (DIL_OUT // LANES) + half
                cols = slice(half * LANES, (half + 1) * LANES)
                for r in range(d):
                    rows = pl.ds(r, tm // d, stride=d) if d > 1 else slice(None)
                    outs[g][0, r, :, cols] = acc[k, rows, :].astype(outs[g].dtype)

    out_specs = [pl.BlockSpec((1, d, tm // d, DIL_OUT), lambda i, j: (j, 0, i, 0)) for _, d in DIL_PAIRS]
    out_shape = [jax.ShapeDtypeStruct((3, d, S // d, DIL_OUT), _CD) for _, d in DIL_PAIRS]
    outs = _pcall(body, name=name, grid=(S // tm, 3),
                  in_specs=[pl.BlockSpec((tm, D), lambda i, j: (i, 0)), pl.BlockSpec((D, tn), lambda i, j: (0, j))],
                  out_specs=out_specs, out_shape=out_shape, scratch_shapes=[pltpu.VMEM((tn // LANES, tm, LANES), F32)],
                  compiler_params=_params("parallel", "arbitrary"))(h, w_qkv)
    return [o.reshape(3, S, DIL_OUT) for o in outs]


def _dil_start(block, S, dilation):
    sub = S // dilation
    u0 = block * DIL_W
    return (u0 % sub) * dilation + u0 // sub


def _dil_slopes(group):
    h = np.arange(1, N_DIL_GROUPS * DIL_HEADS + 1, dtype=np.float32)
    s = (np.float32(2.0) ** (np.float32(-8.0) * h / np.float32(N_DIL_GROUPS * DIL_HEADS))).astype(np.float32)
    return [float(v) for v in s.reshape(N_DIL_GROUPS, DIL_HEADS)[group]]


def _dil_tiles(i, n, blocks_per_seq):
    qi = lax.broadcasted_iota(jnp.int32, (DIL_W, DIL_W), 0)
    kj = lax.broadcasted_iota(jnp.int32, (DIL_W, DIL_W), 1)
    first = ((4 * n + i) % blocks_per_seq) == 0
    valid_prev = jnp.logical_and(kj >= qi, jnp.logical_not(first))
    valid_cur = kj <= qi
    rel_prev = (qi - kj + DIL_W).astype(F32)
    rel_cur = (qi - kj).astype(F32)
    return valid_prev, valid_cur, rel_prev, rel_cur


CHUNK = 4 * DIL_W


def _dil_rows(block, S, dilation):
    start = _dil_start(block, S, dilation)
    return pl.ds(start, DIL_W, stride=dilation) if dilation > 1 else pl.ds(start, DIL_W)


def SPLIT(S):
    return (DIL_OUT // LANES, S, LANES)


def _dil_fwd(qkv, group, *, name):
    S = qkv.shape[1]
    dilation = DIL_PAIRS[group][1]
    bps = (S // dilation) // DIL_W
    slopes = _dil_slopes(group)
    nt = (((1,), (1,)), ((), ()))

    def body(q_ref, k_ref, v_ref, kp_ref, vp_ref, on_ref, ln_ref, o_ref, l_ref):
        n = pl.program_id(0)
        for i in range(4):
            valid_prev, valid_cur, rel_prev, rel_cur = _dil_tiles(i, n, bps)
            rows = slice(i * DIL_W, (i + 1) * DIL_W)
            prow = slice((i - 1) * DIL_W, i * DIL_W)
            for h in range(DIL_HEADS):
                cols = slice(h * HEAD_DIM, (h + 1) * HEAD_DIM)
                qh = q_ref[rows, cols]
                kc, vc = k_ref[rows, cols], v_ref[rows, cols]
                kp = kp_ref[:, cols] if i == 0 else k_ref[prow, cols]
                vp = vp_ref[:, cols] if i == 0 else v_ref[prow, cols]
                sl = slopes[h] * dilation
                sp = lax.dot_general(qh, kp, nt, preferred_element_type=F32) * ATTN_SCALE - sl * rel_prev
                sc = lax.dot_general(qh, kc, nt, preferred_element_type=F32) * ATTN_SCALE - sl * rel_cur
                sp = jnp.where(valid_prev, sp, NEG_INF)
                sc = jnp.where(valid_cur, sc, NEG_INF)
                m = jnp.maximum(jnp.max(sp, axis=-1, keepdims=True), jnp.max(sc, axis=-1, keepdims=True))
                pp, pc = jnp.exp(sp - m), jnp.exp(sc - m)
                den = jnp.sum(pp, axis=-1, keepdims=True) + jnp.sum(pc, axis=-1, keepdims=True)
                acc = (jnp.dot(pp.astype(_CD), vp, preferred_element_type=F32)
                       + jnp.dot(pc.astype(_CD), vc, preferred_element_type=F32))
                o_ref[rows, cols] = acc / den
                l_ref[rows, cols] = jnp.broadcast_to(m + jnp.log(den), (DIL_W, HEAD_DIM))
        for i in range(4):
            rows = slice(i * DIL_W, (i + 1) * DIL_W)
            nat = _dil_rows(4 * n + i, S, dilation)
            for half in range(DIL_OUT // LANES):
                cols = slice(half * LANES, (half + 1) * LANES)
                on_ref[half, nat, :] = o_ref[rows, cols]
                ln_ref[half, nat, :] = l_ref[rows, cols]

    def cur(which):
        return pl.BlockSpec((None, CHUNK, DIL_OUT), lambda n: (which, n, 0))

    def prev(which):
        return pl.BlockSpec((None, DIL_W, DIL_OUT), lambda n: (which, jnp.maximum(4 * n - 1, 0), 0))

    whole = pl.BlockSpec(SPLIT(S), lambda n: (0, 0, 0))
    return _pcall(body, name=name, grid=(S // CHUNK,), in_specs=[cur(0), cur(1), cur(2), prev(1), prev(2)],
                  out_specs=[whole, whole],
                  out_shape=[jax.ShapeDtypeStruct(SPLIT(S), F32), jax.ShapeDtypeStruct(SPLIT(S), F32)],
                  scratch_shapes=[pltpu.VMEM((CHUNK, DIL_OUT), F32), pltpu.VMEM((CHUNK, DIL_OUT), F32)],
                  compiler_params=_params("arbitrary"))(qkv, qkv, qkv, qkv, qkv)


def _dil_bwd(qkv, o, lse, do, dlse, group, *, name):
    S = qkv.shape[1]
    dilation = DIL_PAIRS[group][1]
    bps = (S // dilation) // DIL_W
    slopes = _dil_slopes(group)
    nchunk = S // CHUNK
    nt = (((1,), (1,)), ((), ()))
    tn = (((0,), (0,)), ((), ()))

    def body(q_ref, k_ref, v_ref, kp_ref, vp_ref, on_ref, ln_ref, don_ref, dln_ref, dq_ref, dk_ref, dv_ref,
             dk_s, dv_s, o_ref, l_ref, do_ref, dl_ref):
        step = pl.program_id(0)
        n = nchunk - 1 - step
        for i in range(4):
            rows = slice(i * DIL_W, (i + 1) * DIL_W)
            nat = _dil_rows(4 * n + i, S, dilation)
            for half in range(DIL_OUT // LANES):
                cols = slice(half * LANES, (half + 1) * LANES)
                o_ref[rows, cols] = on_ref[half, nat, :]
                l_ref[rows, cols] = ln_ref[half, nat, :]
                do_ref[rows, cols] = don_ref[half, nat, :]
                dl_ref[rows, cols] = dln_ref[half, nat, :]

        @pl.when(step == 0)
        def _():
            dk_s[CHUNK:, :] = jnp.zeros((DIL_W, DIL_OUT), F32)
            dv_s[CHUNK:, :] = jnp.zeros((DIL_W, DIL_OUT), F32)

        dk_s[:CHUNK, :] = jnp.zeros((CHUNK, DIL_OUT), F32)
        dv_s[:CHUNK, :] = jnp.zeros((CHUNK, DIL_OUT), F32)
        for i in range(4):
            valid_prev, valid_cur, rel_prev, rel_cur = _dil_tiles(i, n, bps)
            rows = slice(i * DIL_W, (i + 1) * DIL_W)
            prow = slice((i - 1) * DIL_W, i * DIL_W)
            s_prev = slice(i * DIL_W, (i + 1) * DIL_W)
            s_cur = slice((i + 1) * DIL_W, (i + 2) * DIL_W)
            for h in range(DIL_HEADS):
                cols = slice(h * HEAD_DIM, (h + 1) * HEAD_DIM)
                qh = q_ref[rows, cols]
                kc, vc = k_ref[rows, cols], v_ref[rows, cols]
                kp = kp_ref[:, cols] if i == 0 else k_ref[prow, cols]
                vp = vp_ref[:, cols] if i == 0 else v_ref[prow, cols]
                sl = slopes[h] * dilation
                lh = l_ref[rows, h * HEAD_DIM:h * HEAD_DIM + 1]
                sp = lax.dot_general(qh, kp, nt, preferred_element_type=F32) * ATTN_SCALE - sl * rel_prev
                sc = lax.dot_general(qh, kc, nt, preferred_element_type=F32) * ATTN_SCALE - sl * rel_cur
                pp = jnp.exp(jnp.where(valid_prev, sp, NEG_INF) - lh)
                pc = jnp.exp(jnp.where(valid_cur, sc, NEG_INF) - lh)
                doh = do_ref[rows, cols]
                dsum = jnp.sum(doh * o_ref[rows, cols], axis=-1, keepdims=True)
                shift = dl_ref[rows, h * HEAD_DIM:h * HEAD_DIM + 1] - dsum
                dob = doh.astype(_CD)
                dsp = pp * (lax.dot_general(dob, vp, nt, preferred_element_type=F32) + shift)
                dsc = pc * (lax.dot_general(dob, vc, nt, preferred_element_type=F32) + shift)
                dspb = (dsp * ATTN_SCALE).astype(_CD)
                dscb = (dsc * ATTN_SCALE).astype(_CD)
                dq_ref[rows, cols] = (jnp.dot(dspb, kp, preferred_element_type=F32)
                                      + jnp.dot(dscb, kc, preferred_element_type=F32)).astype(dq_ref.dtype)
                dk_s[s_prev, cols] += lax.dot_general(dspb, qh, tn, preferred_element_type=F32)
                dk_s[s_cur, cols] += lax.dot_general(dscb, qh, tn, preferred_element_type=F32)
                dv_s[s_prev, cols] += lax.dot_general(pp.astype(_CD), dob, tn, preferred_element_type=F32)
                dv_s[s_cur, cols] += lax.dot_general(pc.astype(_CD), dob, tn, preferred_element_type=F32)
        dk_ref[...] = dk_s[DIL_W:, :].astype(dk_ref.dtype)
        dv_ref[...] = dv_s[DIL_W:, :].astype(dv_ref.dtype)
        dk_s[CHUNK:, :] = dk_s[:DIL_W, :]
        dv_s[CHUNK:, :] = dv_s[:DIL_W, :]

    def cur(which):
        return pl.BlockSpec((None, CHUNK, DIL_OUT), lambda s: (which, nchunk - 1 - s, 0))

    def prev(which):
        return pl.BlockSpec((None, DIL_W, DIL_OUT), lambda s: (which, jnp.maximum(4 * (nchunk - 1 - s) - 1, 0), 0))

    whole = pl.BlockSpec(SPLIT(S), lambda s: (0, 0, 0))
    out = pl.BlockSpec((CHUNK, DIL_OUT), lambda s: (nchunk - 1 - s, 0))
    shp = jax.ShapeDtypeStruct((S, DIL_OUT), _CD)
    tile = pltpu.VMEM((CHUNK, DIL_OUT), F32)
    return _pcall(body, name=name, grid=(nchunk,),
                  in_specs=[cur(0), cur(1), cur(2), prev(1), prev(2), whole, whole, whole, whole],
                  out_specs=[out, out, out], out_shape=[shp, shp, shp],
                  scratch_shapes=[pltpu.VMEM((CHUNK + DIL_W, DIL_OUT), F32), pltpu.VMEM((CHUNK + DIL_W, DIL_OUT), F32),
                                  tile, tile, tile, tile],
                  compiler_params=_params("arbitrary"))(qkv, qkv, qkv, qkv, qkv, o, lse, do, dlse)


def _dil_mix_fwd(os_, ls_, *, name, tm=512):
    nh, S, _ = os_[0].shape

    def body(o0, o1, o2, l0, l1, l2, out_ref):
        for half in range(nh):
            ls = [l0[half], l1[half], l2[half]]
            m = jnp.maximum(jnp.maximum(ls[0], ls[1]), ls[2])
            es = [jnp.exp(l - m) for l in ls]
            den = es[0] + es[1] + es[2]
            mixed = (es[0] * o0[half] + es[1] * o1[half] + es[2] * o2[half]) / den
            out_ref[:, half * LANES:(half + 1) * LANES] = mixed.astype(out_ref.dtype)

    halves = pl.BlockSpec((nh, tm, LANES), lambda i: (0, i, 0))
    row = pl.BlockSpec((tm, nh * LANES), lambda i: (i, 0))
    return _pcall(body, name=name, grid=(S // tm,), in_specs=[halves] * 6, out_specs=row,
                  out_shape=jax.ShapeDtypeStruct((S, nh * LANES), _CD), compiler_params=_params("parallel"))(*os_, *ls_)


def _dil_mix_bwd(doa, os_, ls_, *, name, tm=512, after=None):
    nh, S, _ = os_[0].shape

    def body(d_ref, o0, o1, o2, l0, l1, l2, do0, do1, do2, dl0, dl1, dl2):
        for half in range(nh):
            dv = d_ref[:, half * LANES:(half + 1) * LANES]
            ls = [l0[half], l1[half], l2[half]]
            m = jnp.maximum(jnp.maximum(ls[0], ls[1]), ls[2])
            es = [jnp.exp(l - m) for l in ls]
            den = es[0] + es[1] + es[2]
            al = [e / den for e in es]
            da = [_head_sum(dv * o[half]) for o in (o0, o1, o2)]
            mean = al[0] * da[0] + al[1] * da[1] + al[2] * da[2]
            for a, d_, do_ref, dl_ref in zip(al, da, (do0, do1, do2), (dl0, dl1, dl2)):
                do_ref[half] = a * dv
                dl_ref[half] = a * (d_ - mean)

    halves = pl.BlockSpec((nh, tm, LANES), lambda i: (0, i, 0))
    row = pl.BlockSpec((tm, nh * LANES), lambda i: (i, 0))
    shp = jax.ShapeDtypeStruct((nh, S, LANES), F32)
    return _pcall(body, after, name=name, grid=(S // tm,), in_specs=[row] + [halves] * 6, out_specs=[halves] * 6,
                  out_shape=[shp] * 6, compiler_params=_params("parallel"))(doa, *os_, *ls_)


FOX_T = 512


PACK = 2 * HEAD_DIM
HEAD_PAIRS = N_FOX_HEADS // 2
FOX_HPS = 8
Q_BLOCK0 = 0
K_BLOCK0 = FOX_WIDTH // PACK
V_BLOCK0 = 2 * FOX_WIDTH // PACK


def _pieces(x):
    hi = x.astype(jnp.bfloat16).astype(F32)
    r = x - hi
    mid = r.astype(jnp.bfloat16).astype(F32)
    lo = (r - mid).astype(jnp.bfloat16).astype(F32)
    return [hi, mid, lo]


def _extras(first, second, rows):
    lane = lax.broadcasted_iota(jnp.int32, (rows, HEAD_DIM), 1)
    out = jnp.zeros((rows, HEAD_DIM), F32)
    for idx, val in enumerate(list(first) + list(second)):
        out = jnp.where(lane == idx, val, out)
    return out


def _head_column(c, h):
    lane = lax.broadcasted_iota(jnp.int32, c.shape, 1)
    return jnp.sum(jnp.where(lane == h, c, 0.0), axis=1, keepdims=True)


ONES3 = [1.0, 1.0, 1.0]
ZEROS3 = [0.0, 0.0, 0.0]


def _fox_pack_fwd(qkv, c, *, name, tm=512):
    S = qkv.shape[0]

    def body(q_ref, k_ref, v_ref, c_ref, qo_ref, ko_ref, vo_ref):
        hp = pl.program_id(1)
        cv = c_ref[...]
        for hh in range(2):
            ch = _pieces(_head_column(cv, 2 * hp + hh))
            src = slice(hh * HEAD_DIM, (hh + 1) * HEAD_DIM)
            lo = slice(hh * PACK, hh * PACK + HEAD_DIM)
            hi = slice(hh * PACK + HEAD_DIM, (hh + 1) * PACK)
            qo_ref[:, lo] = (q_ref[:, src].astype(F32) * ATTN_SCALE).astype(qo_ref.dtype)
            qo_ref[:, hi] = _extras(ch, ONES3, tm).astype(qo_ref.dtype)
            ko_ref[:, lo] = k_ref[:, src]
            ko_ref[:, hi] = _extras(ONES3, [-p for p in ch], tm).astype(ko_ref.dtype)
            vo_ref[:, lo] = v_ref[:, src]
            vo_ref[:, hi] = _extras(ONES3, ZEROS3, tm).astype(vo_ref.dtype)

    def src(block0):
        return pl.BlockSpec((tm, PACK), lambda i, hp: (i, block0 + hp))

    out = pl.BlockSpec((tm, 2 * PACK), lambda i, hp: (i, hp))
    shp = jax.ShapeDtypeStruct((S, N_FOX_HEADS * PACK), _CD)
    return _pcall(body, name=name, grid=(S // tm, HEAD_PAIRS),
                  in_specs=[src(Q_BLOCK0), src(K_BLOCK0), src(V_BLOCK0), pl.BlockSpec((tm, PACK), lambda i, hp: (i, 0))],
                  out_specs=[out, out, out], out_shape=[shp, shp, shp],
                  compiler_params=_params("parallel", "parallel"))(qkv, qkv, qkv, c)


def _fox_fwd(qp, kp, vp, *, name):
    S = qp.shape[0]
    nt = S // FOX_T
    nt_dims = (((1,), (1,)), ((), ()))
    tn_dims = (((0,), (0,)), ((), ()))

    def body(i_tab, j_tab, q_ref, k_ref, v_ref, o_ref, l_ref, m_s, acc_s):
        t = pl.program_id(1)
        i, j = i_tab[t], j_tab[t]

        @pl.when(j == 0)
        def _():
            m_s[...] = jnp.full((FOX_HPS, 1, FOX_T), NEG_INF, F32)
            acc_s[...] = jnp.zeros((FOX_HPS, PACK, FOX_T), F32)

        def tile(diagonal):
            for hh in range(FOX_HPS):
                cols = slice(hh * PACK, (hh + 1) * PACK)
                st = lax.dot_general(k_ref[:, cols], q_ref[:, cols], nt_dims, preferred_element_type=F32)
                if diagonal:
                    key = lax.broadcasted_iota(jnp.int32, (FOX_T, FOX_T), 0)
                    qry = lax.broadcasted_iota(jnp.int32, (FOX_T, FOX_T), 1)
                    st = jnp.where(key <= qry, st, NEG_INF)
                m_old = m_s[hh]
                m_new = jnp.maximum(m_old, jnp.max(st, axis=0, keepdims=True))
                pt = jnp.exp(st - m_new)
                acc_s[hh] = jnp.exp(m_old - m_new) * acc_s[hh] + lax.dot_general(
                    v_ref[:, cols], pt.astype(_CD), tn_dims, preferred_element_type=F32)
                m_s[hh] = m_new

        @pl.when(j < i)
        def _():
            tile(False)

        @pl.when(j == i)
        def _():
            tile(True)
            for hh in range(FOX_HPS):
                acc = acc_s[hh]
                den = acc[HEAD_DIM:HEAD_DIM + 1, :]
                cols = slice(hh * HEAD_DIM, (hh + 1) * HEAD_DIM)
                o_ref[:, cols] = (acc[:HEAD_DIM, :] / den).T
                l_ref[:, cols] = jnp.broadcast_to(m_s[hh] + jnp.log(den), (HEAD_DIM, FOX_T)).T

    pairs = [(i, j) for i in range(nt) for j in range(i + 1)]
    i_tab = jnp.asarray([p[0] for p in pairs], jnp.int32)
    j_tab = jnp.asarray([p[1] for p in pairs], jnp.int32)
    qs = pl.BlockSpec((FOX_T, FOX_HPS * PACK), lambda hp, t, it, jt: (it[t], hp))
    ks = pl.BlockSpec((FOX_T, FOX_HPS * PACK), lambda hp, t, it, jt: (jt[t], hp))
    os_ = pl.BlockSpec((FOX_T, FOX_HPS * HEAD_DIM), lambda hp, t, it, jt: (it[t], hp))
    shp = jax.ShapeDtypeStruct((S, FOX_WIDTH), F32)
    grid_spec = pltpu.PrefetchScalarGridSpec(
        num_scalar_prefetch=2, grid=(N_FOX_HEADS // FOX_HPS, len(pairs)), in_specs=[qs, ks, ks], out_specs=[os_, os_],
        scratch_shapes=[pltpu.VMEM((FOX_HPS, 1, FOX_T), F32), pltpu.VMEM((FOX_HPS, PACK, FOX_T), F32)])
    return _pcall(body, name=name, grid_spec=grid_spec, out_shape=[shp, shp],
                  compiler_params=_params("parallel", "arbitrary"))(i_tab, j_tab, qp, kp, vp)


def _fox_pack_bwd(qkv, c, o, lse, do, *, name, tm=512, after=None):
    S = qkv.shape[0]

    def body(q_ref, c_ref, o_ref, l_ref, do_ref, qo_ref, do_out_ref):
        hp = pl.program_id(1)
        cv = c_ref[...]
        for hh in range(2):
            src = slice(hh * HEAD_DIM, (hh + 1) * HEAD_DIM)
            lo = slice(hh * PACK, hh * PACK + HEAD_DIM)
            hi = slice(hh * PACK + HEAD_DIM, (hh + 1) * PACK)
            shift = _head_column(cv, 2 * hp + hh) - l_ref[:, hh * HEAD_DIM:hh * HEAD_DIM + 1]
            dov = do_ref[:, src]
            dsum = jnp.sum(dov * o_ref[:, src], axis=-1, keepdims=True)
            qo_ref[:, lo] = (q_ref[:, src].astype(F32) * ATTN_SCALE).astype(qo_ref.dtype)
            qo_ref[:, hi] = _extras(_pieces(shift), ONES3, tm).astype(qo_ref.dtype)
            do_out_ref[:, lo] = dov.astype(do_out_ref.dtype)
            do_out_ref[:, hi] = _extras(_pieces(-dsum), ZEROS3, tm).astype(do_out_ref.dtype)

    pair = pl.BlockSpec((tm, PACK), lambda i, hp: (i, hp))
    out = pl.BlockSpec((tm, 2 * PACK), lambda i, hp: (i, hp))
    shp = jax.ShapeDtypeStruct((S, N_FOX_HEADS * PACK), _CD)
    return _pcall(body, after, name=name, grid=(S // tm, HEAD_PAIRS),
                  in_specs=[pl.BlockSpec((tm, PACK), lambda i, hp: (i, Q_BLOCK0 + hp)),
                            pl.BlockSpec((tm, PACK), lambda i, hp: (i, 0)), pair, pair, pair],
                  out_specs=[out, out], out_shape=[shp, shp],
                  compiler_params=_params("parallel", "parallel"))(qkv, c, o, lse, do)


def _fox_bwd(qp, kp, vp, dop, *, name):
    S = qp.shape[0]
    nt = S // FOX_T
    nt_dims = (((1,), (1,)), ((), ()))
    tn_dims = (((0,), (0,)), ((), ()))

    def body(i_tab, j_tab, q_ref, k_ref, v_ref, do_ref, dq_ref, dk_ref, dv_ref, dc_ref, dr_ref,
             dq_s, dk_s, dv_s, dc_s, dr_s):
        t = pl.program_id(1)
        i, j = i_tab[t], j_tab[t]

        @pl.when(t == 0)
        def _():
            dq_s[...] = jnp.zeros((S, FOX_HPS * PACK), F32)
            dr_s[...] = jnp.zeros((FOX_HPS, 1, S), F32)

        @pl.when(i == j)
        def _():
            dk_s[...] = jnp.zeros((FOX_T, FOX_HPS * PACK), F32)
            dv_s[...] = jnp.zeros((FOX_T, FOX_HPS * PACK), F32)
            dc_s[...] = jnp.zeros((FOX_HPS, FOX_T, 1), F32)

        def tile(diagonal):
            rows = pl.ds(pl.multiple_of(i * FOX_T, FOX_T), FOX_T)
            for hh in range(FOX_HPS):
                cols = slice(hh * PACK, (hh + 1) * PACK)
                qv, kv, vv, dov = q_ref[:, cols], k_ref[:, cols], v_ref[:, cols], do_ref[:, cols]
                pt = jnp.exp(lax.dot_general(kv, qv, nt_dims, preferred_element_type=F32))
                if diagonal:
                    key = lax.broadcasted_iota(jnp.int32, (FOX_T, FOX_T), 0)
                    qry = lax.broadcasted_iota(jnp.int32, (FOX_T, FOX_T), 1)
                    pt = jnp.where(key <= qry, pt, 0.0)
                dst = pt * lax.dot_general(vv, dov, nt_dims, preferred_element_type=F32)
                dsb = dst.astype(_CD)
                dc_s[hh] += jnp.sum(dst, axis=1, keepdims=True)
                dr_s[hh, :, rows] += jnp.sum(dst, axis=0, keepdims=True)
                dv_s[:, cols] += jnp.dot(pt.astype(_CD), dov, preferred_element_type=F32)
                dk_s[:, cols] += jnp.dot(dsb, qv, preferred_element_type=F32)
                dq_s[rows, cols] += lax.dot_general(dsb, kv, tn_dims, preferred_element_type=F32)

        @pl.when(i > j)
        def _():
            tile(False)

        @pl.when(i == j)
        def _():
            tile(True)

        @pl.when(i == nt - 1)
        def _():
            for hh in range(FOX_HPS):
                src = slice(hh * PACK, hh * PACK + HEAD_DIM)
                dst_cols = slice(hh * HEAD_DIM, (hh + 1) * HEAD_DIM)
                dk_ref[:, dst_cols] = dk_s[:, src].astype(dk_ref.dtype)
                dv_ref[:, dst_cols] = dv_s[:, src].astype(dv_ref.dtype)
                dc_ref[:, dst_cols] = jnp.broadcast_to(dc_s[hh], (FOX_T, HEAD_DIM))

        @pl.when(t == len(pairs) - 1)
        def _():
            for hh in range(FOX_HPS):
                dq_ref[:, hh * HEAD_DIM:(hh + 1) * HEAD_DIM] = (
                    dq_s[:, hh * PACK:hh * PACK + HEAD_DIM] * ATTN_SCALE).astype(dq_ref.dtype)
            dr_ref[...] = dr_s[...]

    pairs = [(i, j) for j in range(nt) for i in range(j, nt)]
    i_tab = jnp.asarray([p[0] for p in pairs], jnp.int32)
    j_tab = jnp.asarray([p[1] for p in pairs], jnp.int32)
    wide, narrow = FOX_HPS * PACK, FOX_HPS * HEAD_DIM
    qs = pl.BlockSpec((FOX_T, wide), lambda hp, t, it, jt: (it[t], hp))
    ks = pl.BlockSpec((FOX_T, wide), lambda hp, t, it, jt: (jt[t], hp))
    whole = pl.BlockSpec((S, narrow), lambda hp, t, it, jt: (0, hp))
    cs = pl.BlockSpec((FOX_T, narrow), lambda hp, t, it, jt: (jt[t], hp))
    rs = pl.BlockSpec((FOX_HPS, 1, S), lambda hp, t, it, jt: (hp, 0, 0))
    shp = jax.ShapeDtypeStruct((S, FOX_WIDTH), _CD)
    grid_spec = pltpu.PrefetchScalarGridSpec(
        num_scalar_prefetch=2, grid=(N_FOX_HEADS // FOX_HPS, len(pairs)), in_specs=[qs, ks, ks, qs],
        out_specs=[whole, cs, cs, cs, rs],
        scratch_shapes=[pltpu.VMEM((S, wide), F32), pltpu.VMEM((FOX_T, wide), F32),
                        pltpu.VMEM((FOX_T, wide), F32), pltpu.VMEM((FOX_HPS, FOX_T, 1), F32),
                        pltpu.VMEM((FOX_HPS, 1, S), F32)])
    return _pcall(body, name=name, grid_spec=grid_spec,
                  out_shape=[shp, shp, shp, jax.ShapeDtypeStruct((S, FOX_WIDTH), F32),
                             jax.ShapeDtypeStruct((N_FOX_HEADS, 1, S), F32)],
                  compiler_params=_params("parallel", "arbitrary"))(i_tab, j_tab, qp, kp, vp, dop)


def _redilate(t, d):
    if d == 1:
        return t
    S, C = t.shape
    return t.reshape(d, S // d, C).transpose(1, 0, 2).reshape(S, C)


def _layer_step(x, tgt, w, p, late_weights=None, grad_sink=None, after=None, first_weights=None):
    S = x.shape[0]
    after_norm, after_proj = after if after is not None else (None, None)
    h = _rms_fwd(x, p["norm_mix_g"], name="rms_mix", after=after_norm)
    if first_weights is not None:
        w = {**w, **first_weights(h)}
    qkv = _mm(h, w["qkv"][:, 3 * DIL_WIDTH:], name="proj_fox", out_dtype=_CD, tn=768, tm=2048, after=after_proj)
    dil_qkv = _proj_dil(h, w["qkv"], name="proj_dil")
    zf = _mm(h, w["f"], name="proj_f")
    gl = _mm(h, w["g"], name="proj_gate", tn=1024)

    dil_o, dil_l = [], []
    for g in range(N_DIL_GROUPS):
        og, lg = _dil_fwd(dil_qkv[g], g, name=f"dil_fwd{g}")
        dil_o.append(og), dil_l.append(lg)
    o_a = _dil_mix_fwd(dil_o, dil_l, name="dil_mix")

    c = _fox_cumsum(zf, p["b_fgt"], name="fox_cumsum")
    fqp, fkp, fvp = _fox_pack_fwd(qkv, c, name="fox_pack")
    o_b, flse = _fox_fwd(fqp, fkp, fvp, name="fox_fwd")

    if late_weights is not None:
        w = {**w, **late_weights(o_b)}
    y_a = _mm(o_a, w["dil_out"], name="y_a", tn=1024, out_dtype=_CD)
    y_b = _mm(o_b, w["fox_out"], name="y_b", tn=1024, out_dtype=_CD)
    merged = _gate_fwd(gl, p["b_gate"], y_a, y_b, name="gate_fwd")
    x1 = _mm(merged, w["out"], name="mix_out", add=x)

    h2 = _rms_fwd(x1, p["norm_ffn_g"], name="rms_ffn")
    gate, up, act = _ffn_in_act(h2, w["ffn_in"], name="ffn_in")
    x2 = _mm(act, w["ffn_down"], name="ffn_down", add=x1, tk=2816)

    loss, dx2, dg_final = _loss_head(x2, p["norm_final_g"], tgt, name="loss_head")

    dact = _mm(dx2, w["ffn_down"], name="d_act", tb=True, tn=1408, out_dtype=_CD)
    gw_ffn_down = _mm(act, dx2, name="gw_ffn_down", ta=True, out_dtype=_CD, tm=1408)
    dgu = _swiglu_bwd(dact, gate, up, name="swiglu_bwd")
    dh2 = _mm(dgu, w["ffn_in"], name="d_h2", tb=True, tk=1408, b_blocks=True, tm=2048)
    gw_ffn_in = _mm(h2, dgu, name="gw_ffn_in", ta=True, out_dtype=_CD, tn=1408, out_blocks=1408)
    sink = grad_sink if grad_sink is not None else (lambda group, grads: None)
    tok = sink("ffn", dict(ffn_in=gw_ffn_in, ffn_down=gw_ffn_down))
    dx1, dg_ffn = _rms_bwd(x1, p["norm_ffn_g"], dh2, dx2, name="rms_ffn_bwd", after=tok)

    dmerged = _mm(dx1, w["out"], name="d_merged", tb=True, out_dtype=_CD)
    gw_out = _mm(merged, dx1, name="gw_out", ta=True, out_dtype=_CD)
    dy_a, dy_b, dgl, db_gate = _gate_bwd(dmerged, gl, p["b_gate"], y_a, y_b, name="gate_bwd")
    do_a = _mm(dy_a, w["dil_out"], name="d_o_a", tb=True)
    gw_dil_out = _mm(o_a, dy_a, name="gw_dil_out", ta=True, out_dtype=_CD, tn=1024)
    do_b = _mm(dy_b, w["fox_out"], name="d_o_b", tb=True)
    gw_fox_out = _mm(o_b, dy_b, name="gw_fox_out", ta=True, out_dtype=_CD, tn=1024)
    tok = sink("mix", dict(dil_out=gw_dil_out, fox_out=gw_fox_out, out=gw_out))

    bqp, bdop = _fox_pack_bwd(qkv, c, o_b, flse, do_b, name="fox_pack_bwd", after=tok)
    dqp, dkp, dvp, dck, dcq = _fox_bwd(bqp, fkp, fvp, bdop, name="fox_bwd")
    dc = dcq[:, 0, :].T - dck.reshape(S, N_FOX_HEADS, HEAD_DIM)[:, :, 0]
    dc = jnp.pad(dc, ((0, 0), (0, F_PAD - N_FOX_HEADS)))
    dzf, db_fgt = _fox_cumsum_bwd(dc, zf, p["b_fgt"], name="fox_cumsum_bwd")

    douts = _dil_mix_bwd(do_a, dil_o, dil_l, name="dil_mix_bwd", after=tok)
    dqs, dks, dvs = [], [], []
    for g, (_, d) in enumerate(DIL_PAIRS):
        dq, dk, dv = _dil_bwd(dil_qkv[g], dil_o[g], dil_l[g], douts[g], douts[3 + g], g, name=f"dil_bwd{g}")
        dqs.append(_redilate(dq, d)), dks.append(_redilate(dk, d)), dvs.append(_redilate(dv, d))
    dqkv = jnp.concatenate(dqs + dks + dvs + [dqp, dkp, dvp], axis=1)

    gw_qkv = _mm(h, dqkv, name="gw_qkv", ta=True, out_dtype=_CD, tn=768)
    gw_g = _mm(h, dgl, name="gw_gate", ta=True, out_dtype=_CD)
    gw_f = _mm(h, dzf, name="gw_f", ta=True, out_dtype=_CD)
    tok = sink("in", dict(qkv=gw_qkv, f=gw_f, g=gw_g))
    dh = _mm(dqkv, w["qkv"], name="d_h_qkv", tb=True, tk=1920, tm=2048, after=tok)
    dh = _mm(dgl, w["g"], name="d_h_gate", tb=True, add=dh)
    dh = _mm(dzf, w["f"], name="d_h_f", tb=True, add=dh)
    dx, dg_mix = _rms_bwd(x, p["norm_mix_g"], dh, dx1, name="rms_mix_bwd")

    gw = dict(qkv=gw_qkv, f=gw_f, g=gw_g, dil_out=gw_dil_out, fox_out=gw_fox_out, out=gw_out, ffn_in=gw_ffn_in,
              ffn_down=gw_ffn_down)
    small = dict(norm_mix_g=dg_mix, b_fgt=db_fgt, b_gate=db_gate, norm_ffn_g=dg_ffn, norm_final_g=dg_final)
    return loss, dx, gw, small


def _position():
    return lax.axis_index("x"), lax.axis_index("y"), lax.axis_index("c")


def _other_chips(x, y):
    return [(1 - x, y), (x, 1 - y), (1 - x, 1 - y)]


ROW_TILE = 16


def _row_chunks(rows, want=4):
    n = want
    while n > 1 and rows % (n * ROW_TILE):
        n //= 2
    return n


SEM_SPEC = pl.BlockSpec(memory_space=pltpu.SEMAPHORE)
ANY_SPEC = pl.BlockSpec(memory_space=pl.ANY)
DATAFLOW = pltpu.SideEffectType.DATAFLOW_SIDE_EFFECTING


def _in_hbm(a):
    return pltpu.with_memory_space_constraint(a, pltpu.HBM)


def _split_copy_start(srcs, land_shapes, copies, after, *, name):
    n, m = len(srcs), len(land_shapes)

    def body(*refs):
        src_refs, land_refs = refs[:n], refs[n:n + m]
        send_sems, recv_sems = refs[n + m + 1], refs[n + m + 2]
        token = refs[-1]
        x, y, c = _position()
        for k, (src, dst, peer) in enumerate(copies(x, y, c, src_refs, land_refs)):
            pltpu.make_async_remote_copy(src_ref=src, dst_ref=dst, send_sem=send_sems.at[k], recv_sem=recv_sems.at[k],
                                         device_id=peer, device_id_type=MESH).start()
        token[...] = jnp.zeros_like(token)

    lands = [lax.empty(s.shape, s.dtype) for s in land_shapes]
    count = len(copies(0, 0, 0, srcs, lands))
    out = _pcall(
        body, name=name,
        out_shape=(pltpu.SemaphoreType.DMA((count,)), pltpu.SemaphoreType.DMA((count,)),
                   *[pltpu.HBM(s.shape, s.dtype) for s in srcs], *[pltpu.HBM(s.shape, s.dtype) for s in land_shapes],
                   jax.ShapeDtypeStruct((8, 128), F32)),
        in_specs=[HBM_SPEC] * (n + m) + [ANY_SPEC],
        out_specs=(SEM_SPEC, SEM_SPEC, *[HBM_SPEC] * (n + m), pl.BlockSpec(memory_space=pltpu.VMEM)),
        input_output_aliases={k: 2 + k for k in range(n + m)},
        compiler_params=pltpu.CompilerParams(has_side_effects=DATAFLOW),
    )(*[_in_hbm(s) for s in srcs], *[_in_hbm(l) for l in lands], after)
    return out[0], out[1], list(out[2:2 + n]), list(out[2 + n:2 + n + m]), out[-1]


def _split_copy_wait(send_sems, recv_sems, srcs, lands, copies, after, *, name):
    n, m = len(srcs), len(lands)

    def body(*refs):
        src_refs, land_refs = refs[:n], refs[n:n + m]
        send, recv = refs[n + m], refs[n + m + 1]
        x, y, c = _position()
        for k, (src, dst, peer) in enumerate(copies(x, y, c, src_refs, land_refs)):
            cp = pltpu.make_async_remote_copy(src_ref=src, dst_ref=dst, send_sem=send.at[k], recv_sem=recv.at[k],
                                              device_id=peer, device_id_type=MESH)
            cp.wait_send()
            cp.wait_recv()

    afters = list(after) if isinstance(after, (list, tuple)) else [after]
    out = _pcall(
        body, name=name,
        out_shape=tuple(pltpu.HBM(s.shape, s.dtype) for s in list(srcs) + list(lands)),
        in_specs=[HBM_SPEC] * (n + m) + [SEM_SPEC, SEM_SPEC] + [ANY_SPEC] * len(afters),
        out_specs=tuple([HBM_SPEC] * (n + m)),
        input_output_aliases={k: k for k in range(n + m)},
        compiler_params=pltpu.CompilerParams(has_side_effects=DATAFLOW),
    )(*srcs, *lands, send_sems, recv_sems, *afters)
    return list(out[:n]), list(out[n:])


def _gather_copies(x, y, c, shard_refs, land_refs):
    out = []
    for s, l in zip(shard_refs, land_refs):
        half = s.shape[0] // 2
        nq = _row_chunks(half)
        for cx, cy in _other_chips(x, y):
            for q in range(nq):
                rows = pl.ds(c * half + q * (half // nq), half // nq)
                out.append((s.at[rows, :], l.at[2 * x + y, rows, :], (cx, cy, c)))
    return out


def _scatter_copies(x, y, c, part_refs, land_refs):
    out = []
    for p, l in zip(part_refs, land_refs):
        nq = _row_chunks(p.shape[1])
        for r, (cx, cy) in enumerate(_other_chips(x, y)):
            for q in range(nq):
                rows = pl.ds(q * (p.shape[1] // nq), p.shape[1] // nq)
                out.append((p.at[2 * cx + cy, rows, :], l.at[r, rows, :], (cx, cy, c)))
    return out


def _forward_halves(lands, *, name):
    n = len(lands)

    def body(*refs):
        ins = refs[:n]
        send_sems, recv_sems = refs[2 * n:]
        x, y, c = _position()
        copies = []
        for w in range(n):
            half = ins[w].shape[1] // 2
            for r, (cx, cy) in enumerate(_other_chips(x, y)):
                blk = ins[w].at[2 * cx + cy, pl.ds(c * half, half), :]
                cp = pltpu.make_async_remote_copy(src_ref=blk, dst_ref=blk, send_sem=send_sems.at[w, r],
                                                  recv_sem=recv_sems.at[w, r], device_id=(x, y, 1 - c),
                                                  device_id_type=MESH)
                cp.start()
                copies.append(cp)
        for w in range(n):
            half = ins[w].shape[1] // 2
            for r, (cx, cy) in enumerate(_other_chips(x, y)):
                blk = ins[w].at[2 * cx + cy, pl.ds((1 - c) * half, half), :]
                pltpu.make_async_remote_copy(src_ref=blk, dst_ref=blk, send_sem=send_sems.at[w, r],
                                             recv_sem=recv_sems.at[w, r], device_id=(x, y, 1 - c),
                                             device_id_type=MESH).wait_recv()
        for cp in copies:
            cp.wait_send()

    return _pcall(
        body, name=name, in_specs=[HBM_SPEC] * n, out_specs=[HBM_SPEC] * n,
        out_shape=[jax.ShapeDtypeStruct(l.shape, l.dtype) for l in lands],
        input_output_aliases={k: k for k in range(n)},
        scratch_shapes=[pltpu.SemaphoreType.DMA((n, 3)), pltpu.SemaphoreType.DMA((n, 3))],
    )(*lands)


def _swap_halves(grads, name="swap_halves"):
    n = len(grads)

    def body(*refs):
        ins, outs = refs[:n], refs[n:2 * n]
        send_sems, recv_sems = refs[2 * n:]
        x, y, c = _position()
        copies = []
        for w in range(n):
            half = ins[w].shape[1] // 2
            cp = pltpu.make_async_remote_copy(
                src_ref=ins[w].at[:, pl.ds((1 - c) * half, half), :], dst_ref=outs[w], send_sem=send_sems.at[w],
                recv_sem=recv_sems.at[w], device_id=(x, y, 1 - c), device_id_type=MESH)
            cp.start()
            copies.append(cp)
        for cp in copies:
            cp.wait()

    return _pcall(
        body, name=name, in_specs=[HBM_SPEC] * n, out_specs=[HBM_SPEC] * n,
        out_shape=[jax.ShapeDtypeStruct((4, g.shape[1] // 2, g.shape[2]), g.dtype) for g in grads],
        scratch_shapes=[pltpu.SemaphoreType.DMA((n,)), pltpu.SemaphoreType.DMA((n,))],
    )(*grads)


def _share_halves(halves):
    n = len(halves)

    def body(*refs):
        ins, outs = refs[:n], refs[n:2 * n]
        send_sems, recv_sems = refs[2 * n:]
        x, y, c = _position()
        copies = []
        for w in range(n):
            cp = pltpu.make_async_remote_copy(src_ref=ins[w], dst_ref=outs[w], send_sem=send_sems.at[w],
                                              recv_sem=recv_sems.at[w], device_id=(x, y, 1 - c), device_id_type=MESH)
            cp.start()
            copies.append(cp)
        for cp in copies:
            cp.wait()

    return _pcall(
        body, name="share_halves", in_specs=[HBM_SPEC] * n, out_specs=[HBM_SPEC] * n,
        out_shape=[jax.ShapeDtypeStruct(h.shape, h.dtype) for h in halves],
        scratch_shapes=[pltpu.SemaphoreType.DMA((n,)), pltpu.SemaphoreType.DMA((n,))],
    )(*halves)


def _sum_small(part):
    rows, width = part.shape

    def body(x_ref, out_ref, all_ref, send_sems, recv_sems):
        x, y, c = _position()
        me, sibling = (x, y, c), (x, y, 1 - c)
        chips = _other_chips(x, y)

        def block(px, py, pc):
            return all_ref.at[pl.ds((4 * px + 2 * py + pc) * rows, rows), :]

        def copy(k, blk, to, src=None):
            return pltpu.make_async_remote_copy(
                src_ref=block(*blk) if src is None else src, dst_ref=block(*blk), send_sem=send_sems.at[k],
                recv_sem=recv_sems.at[k], device_id=to, device_id_type=MESH)

        all_ref[pl.ds((4 * x + 2 * y + c) * rows, rows), :] = x_ref[...]
        first = [copy(0, me, sibling, src=x_ref)]
        first += [copy(1 + j, me, (*chip, c), src=x_ref) for j, chip in enumerate(chips)]
        for cp in first:
            cp.start()
        passed = [copy(4 + j, (*chip, c), sibling) for j, chip in enumerate(chips)]
        for j, chip in enumerate(chips):
            copy(1 + j, (*chip, c), me).wait_recv()
            passed[j].start()
        copy(0, sibling, me).wait_recv()
        for j, chip in enumerate(chips):
            copy(4 + j, (*chip, 1 - c), me).wait_recv()
        for cp in first + passed:
            cp.wait_send()
        total = all_ref[0:rows, :]
        for d in range(1, 8):
            total = total + all_ref[d * rows:(d + 1) * rows, :]
        out_ref[...] = total

    vm = pl.BlockSpec(memory_space=pltpu.VMEM)
    return _pcall(
        body, name="sum_small", in_specs=[vm], out_specs=vm, out_shape=jax.ShapeDtypeStruct((rows, width), F32),
        scratch_shapes=[pltpu.VMEM((8 * rows, width), F32), pltpu.SemaphoreType.DMA((7,)), pltpu.SemaphoreType.DMA((7,))],
    )(part)


def _row_tile(R, C, itemsize=4, budget=1 << 20):
    for t in (512, 256, 128, 64, 32, 16, 8):
        if R % t == 0 and t * C * itemsize <= budget:
            return t
    return R


def _add_halves(g, recv, c, *, name):
    _, R, C = g.shape
    half = R // 2
    t = _row_tile(half, C)
    nb = half // t

    def body(c_ref, g_ref, r_ref, o_ref):
        o_ref[...] = (g_ref[...].astype(F32) + r_ref[...].astype(F32)).astype(o_ref.dtype)

    grid_spec = pltpu.PrefetchScalarGridSpec(
        num_scalar_prefetch=1, grid=(4, nb),
        in_specs=[pl.BlockSpec((1, t, C), lambda k, i, cr: (k, cr[0] * nb + i, 0)),
                  pl.BlockSpec((1, t, C), lambda k, i, cr: (k, i, 0))],
        out_specs=pl.BlockSpec((1, t, C), lambda k, i, cr: (k, i, 0)))
    return _pcall(body, name=name, grid_spec=grid_spec, out_shape=jax.ShapeDtypeStruct((4, half, C), g.dtype),
                  compiler_params=_params("parallel", "parallel"))(c, g, recv)


def _add_owners(mine, recv, *, name):
    half, C = mine.shape
    t = _row_tile(half, C)

    def body(m_ref, r_ref, o_ref):
        o_ref[...] = ((m_ref[...].astype(F32) + r_ref[0].astype(F32)) + r_ref[1].astype(F32)) + r_ref[2].astype(F32)

    return _pcall(body, name=name, grid=(half // t,),
                  in_specs=[pl.BlockSpec((t, C), lambda i: (i, 0)), pl.BlockSpec((3, t, C), lambda i: (0, i, 0))],
                  out_specs=pl.BlockSpec((t, C), lambda i: (i, 0)), out_shape=jax.ShapeDtypeStruct((half, C), F32),
                  compiler_params=_params("parallel"))(mine, recv)


def _adamw(w, g, m, v, *, name):
    R, C = w.shape
    t = _row_tile(R, C)
    c1 = 1.0 - ADAM_B1 ** ADAM_STEP
    c2 = 1.0 - ADAM_B2 ** ADAM_STEP

    def body(w_ref, g_ref, m_ref, v_ref, d_ref, nm_ref, nv_ref):
        gv = g_ref[...]
        mn = ADAM_B1 * m_ref[...] + (1.0 - ADAM_B1) * gv
        vn = ADAM_B2 * v_ref[...] + (1.0 - ADAM_B2) * (gv * gv)
        d_ref[...] = -ADAM_LR * ((mn / c1) / (jnp.sqrt(vn / c2) + ADAM_EPS) + ADAM_WD * w_ref[...])
        nm_ref[...] = mn
        nv_ref[...] = vn

    blk = pl.BlockSpec((t, C), lambda i: (i, 0))
    shp = jax.ShapeDtypeStruct((R, C), F32)
    return _pcall(body, name=name, grid=(R // t,), in_specs=[blk] * 4, out_specs=[blk] * 3, out_shape=[shp] * 3,
                  compiler_params=_params("parallel"))(w, g, m, v)


BIG = ("w_in", "w_dil_out", "w_fox_out", "w_out", "w_ffn_in", "w_ffn_down")
SMALL = ("norm_mix_g", "b_fgt", "b_gate", "norm_ffn_g", "norm_final_g")
ORDER = ("norm_mix_g", "w_in", "b_fgt", "b_gate", "w_dil_out", "w_fox_out", "w_out", "norm_ffn_g", "w_ffn_in",
         "w_ffn_down", "norm_final_g")
SMALL_ROWS = {"norm_mix_g": (0, 1), "b_gate": (1, 3), "norm_ffn_g": (3, 4), "norm_final_g": (4, 5), "b_fgt": (5, 6)}


def _columns_to_blocks(full, ncol):
    K = full.shape[0]
    return full.reshape(K, 4, ncol).transpose(1, 0, 2)


def _blocks_to_columns(blocks):
    n, K, ncol = blocks.shape
    return blocks.transpose(1, 0, 2).reshape(K, n * ncol)


def kernel(x, norm_mix_g, w_in, b_fgt, b_gate, w_dil_out, w_fox_out, w_out, norm_ffn_g, w_ffn_in, w_ffn_down, norm_final_g, loss_target, m_norm_mix_g, m_w_in, m_b_fgt, m_b_gate, m_w_dil_out, m_w_fox_out, m_w_out, m_norm_ffn_g, m_w_ffn_in, m_w_ffn_down, m_norm_final_g, v_norm_mix_g, v_w_in, v_b_fgt, v_b_gate, v_w_dil_out, v_w_fox_out, v_w_out, v_norm_ffn_g, v_w_ffn_in, v_w_ffn_down, v_norm_final_g):
    weights = dict(norm_mix_g=norm_mix_g, w_in=w_in, b_fgt=b_fgt, b_gate=b_gate, w_dil_out=w_dil_out,
                   w_fox_out=w_fox_out, w_out=w_out, norm_ffn_g=norm_ffn_g, w_ffn_in=w_ffn_in, w_ffn_down=w_ffn_down,
                   norm_final_g=norm_final_g)
    m_in = dict(norm_mix_g=m_norm_mix_g, w_in=m_w_in, b_fgt=m_b_fgt, b_gate=m_b_gate, w_dil_out=m_w_dil_out,
                w_fox_out=m_w_fox_out, w_out=m_w_out, norm_ffn_g=m_norm_ffn_g, w_ffn_in=m_w_ffn_in,
                w_ffn_down=m_w_ffn_down, norm_final_g=m_norm_final_g)
    v_in = dict(norm_mix_g=v_norm_mix_g, w_in=v_w_in, b_fgt=v_b_fgt, b_gate=v_b_gate, w_dil_out=v_w_dil_out,
                w_fox_out=v_w_fox_out, w_out=v_w_out, norm_ffn_g=v_norm_ffn_g, w_ffn_in=v_w_ffn_in,
                w_ffn_down=v_w_ffn_down, norm_final_g=v_norm_final_g)
    c = lax.axis_index("c")
    chip = 2 * lax.axis_index("x") + lax.axis_index("y")

    shards = {n: weights[n][0].astype(_CD) for n in BIG}
    in_shape = jax.ShapeDtypeStruct((4,) + shards["w_in"].shape, _CD)
    send_i, recv_i, in_src, in_land, token_in = _split_copy_start(
        [shards["w_in"]], [in_shape], _gather_copies, norm_mix_g, name="gather_in_start")
    late = BIG[1:]
    send_g, recv_g, late_src, late_land, token = _split_copy_start(
        [shards[n] for n in late], [jax.ShapeDtypeStruct((4,) + shards[n].shape, _CD) for n in late],
        _gather_copies, token_in, name="gather_late_start")
    adam_in = [t[0] + token_in[0, 0] for t in (w_in, m_w_in, v_w_in)]
    p = dict(norm_mix_g=norm_mix_g, b_fgt=jnp.pad(b_fgt, ((0, 0), (0, F_PAD - N_FOX_HEADS))), b_gate=b_gate,
             norm_ffn_g=norm_ffn_g, norm_final_g=norm_final_g.reshape(1, D_MODEL))

    def first_weights(after):
        own, lands = _split_copy_wait(send_i, recv_i, in_src, in_land, _gather_copies, [after] + adam_in,
                                      name="gather_in_wait")
        (g_in,) = _forward_halves(lands, name="gather_in_forward")
        full_in = _blocks_to_columns(lax.dynamic_update_index_in_dim(g_in, own[0], chip, 0))
        o3 = QKV_COLS
        o4 = o3 + N_FOX_HEADS
        return dict(qkv=full_in[:, :o3], f=jnp.pad(full_in[:, o3:o4], ((0, 0), (0, F_PAD - N_FOX_HEADS))),
                    g=full_in[:, o4:])

    def late_weights(after):
        own, lands = _split_copy_wait(send_g, recv_g, late_src, late_land, _gather_copies, after,
                                      name="gather_late_wait")
        lands = _forward_halves(lands, name="gather_late_forward")
        g_dil, g_fox, g_out, g_ffn_in, g_ffn_down = [
            lax.dynamic_update_index_in_dim(l, s, chip, 0) for l, s in zip(lands, own)]
        return dict(dil_out=_blocks_to_columns(g_dil), fox_out=_blocks_to_columns(g_fox),
                    out=g_out.reshape(D_MODEL, D_MODEL), ffn_in=g_ffn_in,
                    ffn_down=g_ffn_down.reshape(D_FF, D_MODEL))

    c_arr = jnp.reshape(c, (1,)).astype(jnp.int32)

    def to_blocks(n, full):
        shape = weights[n].shape
        if full.ndim == 3:
            return full
        if n in ("w_out", "w_ffn_down"):
            return full.reshape(4, shape[1], shape[2])
        return _columns_to_blocks(full, shape[2])

    def pair_sums(group, named):
        names = list(named)
        blocks = [to_blocks(n, named[n]) for n in names]
        from_sibling = _swap_halves(blocks, name=f"swap_halves_{group}")
        return [_add_halves(b, r, c_arr, name=f"add_halves_{n}") for b, r, n in zip(blocks, from_sibling, names)]

    in_flight = {}

    def grad_sink(group, gw):
        if group == "in":
            named = {"w_in": jnp.concatenate([gw["qkv"], gw["f"][:, :N_FOX_HEADS], gw["g"]], axis=1)}
        else:
            named = {"w_" + k: v for k, v in gw.items()}
        sums = pair_sums(group, named)
        started = _split_copy_start(sums, [jax.ShapeDtypeStruct((3,) + s.shape[1:], s.dtype) for s in sums],
                                    _scatter_copies, next(iter(gw.values())), name=f"scatter_{group}_start")
        in_flight[group] = (list(named), started)
        return started[-1]

    loss_part, grad_x, gw, small = _layer_step(x[0], loss_target[0], {}, p, late_weights, grad_sink,
                                               (token_in, token), first_weights)

    def owner_sums(names, sums, from_chips):
        return {n: _add_owners(lax.dynamic_index_in_dim(s, chip, 0, keepdims=False), r, name=f"add_owners_{n}")
                for n, s, r in zip(names, sums, from_chips)}

    halves = {}
    for group, (names, (send_s, recv_s, srcs, lands, _)) in in_flight.items():
        sums, from_chips = _split_copy_wait(send_s, recv_s, srcs, lands, _scatter_copies, grad_x,
                                            name=f"scatter_{group}_wait")
        halves.update(owner_sums(names, sums, from_chips))
    halves = [halves[n] for n in BIG]
    grads = {}
    for n, own, other in zip(BIG, halves, _share_halves(halves)):
        pair = jnp.stack([own, other])
        grads[n] = jnp.where(c == 0, pair, pair[::-1]).reshape(2 * own.shape[0], own.shape[1])

    packed = jnp.concatenate([
        small["norm_mix_g"], small["b_gate"].reshape(2, D_MODEL), small["norm_ffn_g"], small["norm_final_g"],
        jnp.pad(small["b_fgt"], ((0, 0), (0, D_MODEL - F_PAD))), jnp.pad(loss_part, ((0, 0), (0, D_MODEL - 1))),
        jnp.zeros((1, D_MODEL), F32)], axis=0)
    summed = _sum_small(packed)
    for n in SMALL:
        lo, hi = SMALL_ROWS[n]
        grads[n] = summed[lo:hi].reshape(1, -1)[:, :weights[n].size]
    loss = summed[6, 0]

    out_g, out_d, out_m, out_v = {}, {}, {}, {}
    for n in ORDER:
        shape = weights[n].shape
        two_d = shape[1:] if len(shape) == 3 else (1, weights[n].size)
        g2 = grads[n].reshape(two_d)
        wmv = adam_in if n == "w_in" else [t.reshape(two_d) for t in (weights[n], m_in[n], v_in[n])]
        d2, m2, v2 = _adamw(wmv[0], g2, wmv[1], wmv[2], name=f"adamw_{n}")
        out_g[n], out_d[n], out_m[n], out_v[n] = (g2.reshape(shape), d2.reshape(shape), m2.reshape(shape),
                                                  v2.reshape(shape))
    return (loss, grad_x[None], *[out_g[n] for n in ORDER], *[out_d[n] for n in ORDER],
            *[out_m[n] for n in ORDER], *[out_v[n] for n in ORDER])
```

```python
import numpy as np
import jax
import jax.numpy as jnp
from jax import lax
from jax.experimental import pallas as pl
from jax.experimental.pallas import tpu as pltpu

F32 = jnp.float32
_CD = jnp.bfloat16

D_MODEL = 1024
HEAD_DIM = 64
DIL_PAIRS = ((128, 1), (512, 4), (2048, 16))
N_DIL_GROUPS = 3
DIL_HEADS = 4
DIL_W = 128
DIL_OUT = DIL_HEADS * HEAD_DIM
DIL_WIDTH = N_DIL_GROUPS * DIL_OUT
N_FOX_HEADS = 8
FOX_WIDTH = N_FOX_HEADS * HEAD_DIM
D_FF = 2816
QKV_COLS = 3 * DIL_WIDTH + 3 * FOX_WIDTH
F_PAD = 128
RMS_EPS = 1e-6
NEG_INF = -1e30
ATTN_SCALE = HEAD_DIM ** -0.5
ADAM_LR, ADAM_B1, ADAM_B2, ADAM_EPS, ADAM_WD, ADAM_STEP = 0.001, 0.9, 0.999, 1e-08, 0.01, 10

VMEM_LIMIT = 48 * 1024 * 1024
LANES = 128
MESH = pl.DeviceIdType.MESH
HBM_SPEC = pl.BlockSpec(memory_space=pltpu.HBM)


def _pcall(body, after=None, **kw):
    if after is None:
        return pl.pallas_call(body, **kw)
    n_in = len(kw["in_specs"])
    kw["in_specs"] = list(kw["in_specs"]) + [pl.BlockSpec(memory_space=pl.ANY)]

    def tied(*refs):
        return body(*refs[:n_in], *refs[n_in + 1:])

    call = pl.pallas_call(tied, **kw)
    return lambda *args: call(*args, after)


def _params(*sem):
    return pltpu.CompilerParams(dimension_semantics=sem, vmem_limit_bytes=VMEM_LIMIT)


def _pick(dim, pref):
    t = (min(pref, dim) // 128) * 128
    while t >= 128:
        if dim % t == 0:
            return t
        t -= 128
    return dim


def _mm(a, b, *, name, ta=False, tb=False, out_dtype=F32, add=None, tm=1024, tn=512, tk=2048, after=None,
        b_blocks=False, out_blocks=None, a_halves=False, b_halves=False):
    if a_halves:
        M, K = a.shape[1], 2 * a.shape[2]
    elif ta:
        K, M = a.shape
    else:
        M, K = a.shape
    if b_halves:
        b_rows, b_cols = b.shape[1], 2 * b.shape[2]
    else:
        b_rows, b_cols = (b.shape[1], b.shape[0] * b.shape[2]) if b_blocks else b.shape
    if tb:
        N, K2 = b_rows, b_cols
    else:
        K2, N = b_rows, b_cols
    assert K == K2, (a.shape, b.shape)
    shard = b.shape[2] if b_blocks else None
    tm = _pick(M, tm)
    tn = _pick(shard if (b_blocks and not tb) else (out_blocks or N), tn)
    tk = _pick(shard if (b_blocks and tb) else K, tk)
    nk = K // tk
    dn = (((0 if ta else 1,), (1 if tb else 0,)), ((), ()))
    has_add = add is not None
    assert not (has_add and out_blocks)

    def body(*refs):
        a_ref, b_ref = refs[0], refs[1]
        add_ref = refs[2] if has_add else None
        o_ref = refs[3] if has_add else refs[2]
        bv = b_ref[0] if b_blocks else b_ref[...]
        p = lax.dot_general(a_ref[...].astype(_CD), bv.astype(_CD), dn, preferred_element_type=F32)

        def finish(r):
            if has_add:
                r = r + add_ref[...]
            if out_blocks:
                o_ref[0] = r.astype(out_dtype)
            else:
                o_ref[...] = r.astype(out_dtype)

        if nk == 1:
            finish(p)
        else:
            acc_ref = refs[-1]
            k = pl.program_id(2)

            @pl.when(k == 0)
            def _():
                acc_ref[...] = p

            @pl.when(k > 0)
            def _():
                acc_ref[...] += p

            @pl.when(k == nk - 1)
            def _():
                finish(acc_ref[...])

    if a_halves:
        ka = (K // 2) // tk
        a_spec = pl.BlockSpec((None, tm, tk), lambda i, j, k: (k // ka, i, k % ka))
    else:
        a_spec = pl.BlockSpec((tk, tm), lambda i, j, k: (k, i)) if ta else pl.BlockSpec((tm, tk), lambda i, j, k: (i, k))
    if b_halves:
        nb_ = (N // 2) // tn
        b_spec = pl.BlockSpec((None, tk, tn), lambda i, j, k: (j // nb_, k, j % nb_))
    elif b_blocks and tb:
        per = shard // tk
        b_spec = pl.BlockSpec((1, tn, tk), lambda i, j, k: (k // per, j, k % per))
    elif b_blocks:
        per = shard // tn
        b_spec = pl.BlockSpec((1, tk, tn), lambda i, j, k: (j // per, k, j % per))
    else:
        b_spec = pl.BlockSpec((tn, tk), lambda i, j, k: (j, k)) if tb else pl.BlockSpec((tk, tn), lambda i, j, k: (k, j))
    if out_blocks:
        oper = out_blocks // tn
        o_spec = pl.BlockSpec((1, tm, tn), lambda i, j, k: (j // oper, i, j % oper))
        out_shape = jax.ShapeDtypeStruct((N // out_blocks, M, out_blocks), out_dtype)
    else:
        o_spec = pl.BlockSpec((tm, tn), lambda i, j, k: (i, j))
        out_shape = jax.ShapeDtypeStruct((M, N), out_dtype)
    in_specs = [a_spec, b_spec] + ([o_spec] if has_add else [])
    args = (a, b) + ((add,) if has_add else ())
    return _pcall(
        body, after, name=name, grid=(M // tm, N // tn, nk), in_specs=in_specs, out_specs=o_spec,
        out_shape=out_shape,
        scratch_shapes=[pltpu.VMEM((tm, tn), F32)] if nk > 1 else [],
        compiler_params=_params("parallel", "parallel", "arbitrary"),
    )(*args)


def _rms_fwd(x, g, *, name, tm=512, after=None):
    S, D = x.shape

    def body(x_ref, g_ref, h_ref):
        xv = x_ref[...]
        r = lax.rsqrt(jnp.mean(xv * xv, axis=-1, keepdims=True) + RMS_EPS)
        h_ref[...] = ((xv * r) * g_ref[...]).astype(h_ref.dtype)

    row = pl.BlockSpec((tm, D), lambda i: (i, 0))
    return _pcall(body, after, name=name, grid=(S // tm,), in_specs=[row, pl.BlockSpec((1, D), lambda i: (0, 0))],
                  out_specs=row, out_shape=jax.ShapeDtypeStruct((S, D), _CD), compiler_params=_params("parallel"))(x, g)


def _rms_bwd(x, g, dh, dres, *, name, tm=512, after=None):
    S, D = x.shape

    def body(x_ref, g_ref, dh_ref, dres_ref, dx_ref, dg_ref):
        xv = x_ref[...]
        r = lax.rsqrt(jnp.mean(xv * xv, axis=-1, keepdims=True) + RMS_EPS)
        xh = xv * r
        dhv = dh_ref[...]
        dxh = dhv * g_ref[...]
        dx_ref[...] = dres_ref[...] + r * (dxh - xh * jnp.mean(dxh * xh, axis=-1, keepdims=True))
        part = jnp.sum(dhv * xh, axis=0, keepdims=True)

        @pl.when(pl.program_id(0) == 0)
        def _():
            dg_ref[...] = part

        @pl.when(pl.program_id(0) > 0)
        def _():
            dg_ref[...] += part

    row = pl.BlockSpec((tm, D), lambda i: (i, 0))
    vec = pl.BlockSpec((1, D), lambda i: (0, 0))
    return _pcall(body, after, name=name, grid=(S // tm,), in_specs=[row, vec, row, row], out_specs=[row, vec],
                  out_shape=[jax.ShapeDtypeStruct((S, D), F32), jax.ShapeDtypeStruct((1, D), F32)],
                  compiler_params=_params("arbitrary"))(x, g, dh, dres)


def _loss_head(x, g, tgt, *, name, tm=512):
    S, D = x.shape

    def body(x_ref, g_ref, t_ref, loss_ref, dx_ref, dg_ref):
        xv = x_ref[...]
        gv = g_ref[...]
        r = lax.rsqrt(jnp.mean(xv * xv, axis=-1, keepdims=True) + RMS_EPS)
        xh = xv * r
        err = xh * gv - t_ref[...]
        lpart = 0.5 * jnp.sum(jnp.mean(err * err, axis=-1, keepdims=True), axis=0, keepdims=True)
        dy = err * (1.0 / D)
        dxh = dy * gv
        dx_ref[...] = r * (dxh - xh * jnp.mean(dxh * xh, axis=-1, keepdims=True))
        gpart = jnp.sum(dy * xh, axis=0, keepdims=True)

        @pl.when(pl.program_id(0) == 0)
        def _():
            loss_ref[...] = lpart
            dg_ref[...] = gpart

        @pl.when(pl.program_id(0) > 0)
        def _():
            loss_ref[...] += lpart
            dg_ref[...] += gpart

    row = pl.BlockSpec((tm, D), lambda i: (i, 0))
    vec = pl.BlockSpec((1, D), lambda i: (0, 0))
    one = pl.BlockSpec((1, 1), lambda i: (0, 0))
    return _pcall(body, name=name, grid=(S // tm,), in_specs=[row, vec, row], out_specs=[one, row, vec],
                  out_shape=[jax.ShapeDtypeStruct((1, 1), F32), jax.ShapeDtypeStruct((S, D), F32),
                             jax.ShapeDtypeStruct((1, D), F32)],
                  compiler_params=_params("arbitrary"))(x, g, tgt)


def _sigmoid(z):
    return 1.0 / (1.0 + jnp.exp(-z))


def _gate_fwd(gl, bg, ya, yb, *, name, tm=512):
    S, D = ya.shape

    def body(za_ref, zb_ref, ba_ref, bb_ref, ya_ref, yb_ref, o_ref):
        ga = _sigmoid(za_ref[...].astype(F32) + ba_ref[...])
        gb = _sigmoid(zb_ref[...].astype(F32) + bb_ref[...])
        o_ref[...] = (ga * ya_ref[...].astype(F32) + gb * yb_ref[...].astype(F32)).astype(o_ref.dtype)

    lo = pl.BlockSpec((tm, D), lambda i: (i, 0))
    hi = pl.BlockSpec((tm, D), lambda i: (i, 1))
    vlo = pl.BlockSpec((1, D), lambda i: (0, 0))
    vhi = pl.BlockSpec((1, D), lambda i: (0, 1))
    return _pcall(body, name=name, grid=(S // tm,), in_specs=[lo, hi, vlo, vhi, lo, lo], out_specs=lo,
                  out_shape=jax.ShapeDtypeStruct((S, D), _CD), compiler_params=_params("parallel"))(gl, gl, bg, bg, ya, yb)


def _gate_bwd(dm, gl, bg, ya, yb, *, name, tm=256):
    S, D = ya.shape

    def body(dm_ref, za_ref, zb_ref, ba_ref, bb_ref, ya_ref, yb_ref, dya_ref, dyb_ref, dgl_ref, dbg_ref):
        dmv = dm_ref[...].astype(F32)
        ga = _sigmoid(za_ref[...].astype(F32) + ba_ref[...])
        gb = _sigmoid(zb_ref[...].astype(F32) + bb_ref[...])
        dya_ref[...] = (dmv * ga).astype(dya_ref.dtype)
        dyb_ref[...] = (dmv * gb).astype(dyb_ref.dtype)
        dza = dmv * ya_ref[...].astype(F32) * ga * (1.0 - ga)
        dzb = dmv * yb_ref[...].astype(F32) * gb * (1.0 - gb)
        dgl_ref[:, :D] = dza.astype(dgl_ref.dtype)
        dgl_ref[:, D:] = dzb.astype(dgl_ref.dtype)
        pa = jnp.sum(dza, axis=0, keepdims=True)
        pb = jnp.sum(dzb, axis=0, keepdims=True)

        @pl.when(pl.program_id(0) == 0)
        def _():
            dbg_ref[:, :D] = pa
            dbg_ref[:, D:] = pb

        @pl.when(pl.program_id(0) > 0)
        def _():
            dbg_ref[:, :D] += pa
            dbg_ref[:, D:] += pb

    lo = pl.BlockSpec((tm, D), lambda i: (i, 0))
    hi = pl.BlockSpec((tm, D), lambda i: (i, 1))
    vlo = pl.BlockSpec((1, D), lambda i: (0, 0))
    vhi = pl.BlockSpec((1, D), lambda i: (0, 1))
    wide = pl.BlockSpec((tm, 2 * D), lambda i: (i, 0))
    vwide = pl.BlockSpec((1, 2 * D), lambda i: (0, 0))
    return _pcall(body, name=name, grid=(S // tm,), in_specs=[lo, lo, hi, vlo, vhi, lo, lo],
                  out_specs=[lo, lo, wide, vwide],
                  out_shape=[jax.ShapeDtypeStruct((S, D), _CD), jax.ShapeDtypeStruct((S, D), _CD),
                             jax.ShapeDtypeStruct((S, 2 * D), _CD), jax.ShapeDtypeStruct((1, 2 * D), F32)],
                  compiler_params=_params("arbitrary"))(dm, gl, gl, bg, bg, ya, yb)


def _ffn_in_act(h2, w_blocks, *, name, tm=512):
    S, D = h2.shape
    _, _, C = w_blocks.shape

    def body(a_ref, bg_ref, bu_ref, g_ref, u_ref, o_ref):
        av = a_ref[...].astype(_CD)
        gv = jnp.dot(av, bg_ref[0].astype(_CD), preferred_element_type=F32)
        uv = jnp.dot(av, bu_ref[0].astype(_CD), preferred_element_type=F32)
        g_ref[...] = gv.astype(g_ref.dtype)
        u_ref[...] = uv.astype(u_ref.dtype)
        o_ref[...] = (gv * _sigmoid(gv) * uv).astype(o_ref.dtype)

    out = pl.BlockSpec((tm, C), lambda i, j: (i, j))
    shp = jax.ShapeDtypeStruct((S, 2 * C), _CD)
    return _pcall(body, name=name, grid=(S // tm, 2),
                  in_specs=[pl.BlockSpec((tm, D), lambda i, j: (i, 0)), pl.BlockSpec((1, D, C), lambda i, j: (j, 0, 0)),
                            pl.BlockSpec((1, D, C), lambda i, j: (2 + j, 0, 0))],
                  out_specs=[out, out, out], out_shape=[shp, shp, shp],
                  compiler_params=_params("parallel", "arbitrary"))(h2, w_blocks, w_blocks)


def _d_swiglu(dx, w_down, gate, up, *, name, tm=512, tn=1408):
    S, D = dx.shape
    F = w_down.shape[0]
    nt = (((1,), (1,)), ((), ()))

    def body(a_ref, b_ref, g_ref, u_ref, o_ref):
        dv = lax.dot_general(a_ref[...].astype(_CD), b_ref[...].astype(_CD), nt, preferred_element_type=F32)
        gv = g_ref[...].astype(F32)
        sg = _sigmoid(gv)
        o_ref[0] = (dv * u_ref[...].astype(F32) * (sg * (1.0 + gv * (1.0 - sg)))).astype(o_ref.dtype)
        o_ref[1] = (dv * (gv * sg)).astype(o_ref.dtype)

    tile = pl.BlockSpec((tm, tn), lambda i, j: (i, j))
    return _pcall(body, name=name, grid=(S // tm, F // tn),
                  in_specs=[pl.BlockSpec((tm, D), lambda i, j: (i, 0)), pl.BlockSpec((tn, D), lambda i, j: (j, 0)),
                            tile, tile],
                  out_specs=pl.BlockSpec((2, tm, tn), lambda i, j: (0, i, j)),
                  out_shape=jax.ShapeDtypeStruct((2, S, F), _CD),
                  compiler_params=_params("parallel", "arbitrary"))(dx, w_down, gate, up)


def _split3(x):
    hi = x.astype(jnp.bfloat16)
    r1 = x - hi.astype(F32)
    mid = r1.astype(jnp.bfloat16)
    lo = (r1 - mid.astype(F32)).astype(jnp.bfloat16)
    return hi, mid, lo


def _ones_dot_left(ones, x):
    return sum(jnp.dot(ones, p, preferred_element_type=F32) for p in _split3(x))


def _ones_dot_right(x, ones):
    return sum(jnp.dot(p, ones, preferred_element_type=F32) for p in _split3(x))


def _head_sum(x):
    n = x.shape[1]
    r = lax.broadcasted_iota(jnp.int32, (n, n), 0) // HEAD_DIM
    c = lax.broadcasted_iota(jnp.int32, (n, n), 1) // HEAD_DIM
    return _ones_dot_right(x, (r == c).astype(jnp.bfloat16))


def _log_sigmoid(z):
    e = jnp.exp(-jnp.abs(z))
    t = 1.0 + e
    log1p_e = jnp.where(t == 1.0, e, jnp.log(t) * (e / jnp.where(t == 1.0, 1.0, t - 1.0)))
    return jnp.minimum(z, 0.0) - log1p_e


def _fox_cumsum(zf, bf, *, name):
    S, W = zf.shape
    nb = S // 128

    def body(z_ref, b_ref, c_ref):
        tri = (lax.broadcasted_iota(jnp.int32, (128, 128), 0) >= lax.broadcasted_iota(jnp.int32, (128, 128), 1))
        tri = tri.astype(jnp.bfloat16)

        def step(i, carry):
            rows = pl.ds(pl.multiple_of(i * 128, 128), 128)
            lf = _log_sigmoid(z_ref[rows, :] + b_ref[...])
            cb = _ones_dot_left(tri, lf) + carry
            c_ref[rows, :] = cb
            return cb[127:128, :]

        lax.fori_loop(0, nb, step, jnp.zeros((1, W), F32))

    return _pcall(body, name=name, out_shape=jax.ShapeDtypeStruct((S, W), F32),
                  compiler_params=pltpu.CompilerParams(vmem_limit_bytes=VMEM_LIMIT))(zf, bf)


def _fox_cumsum_bwd(dc, zf, bf, *, name):
    S, W = zf.shape
    nb = S // 128

    def body(dc_ref, z_ref, b_ref, dz_ref, db_ref):
        tri = (lax.broadcasted_iota(jnp.int32, (128, 128), 0) <= lax.broadcasted_iota(jnp.int32, (128, 128), 1))
        tri = tri.astype(jnp.bfloat16)

        def step(k, carry):
            tail, acc = carry
            i = nb - 1 - k
            rows = pl.ds(pl.multiple_of(i * 128, 128), 128)
            dlf = _ones_dot_left(tri, dc_ref[rows, :]) + tail
            dz = dlf * _sigmoid(-(z_ref[rows, :] + b_ref[...]))
            dz_ref[rows, :] = dz
            return dlf[0:1, :], acc + jnp.sum(dz, axis=0, keepdims=True)

        _, acc = lax.fori_loop(0, nb, step, (jnp.zeros((1, W), F32), jnp.zeros((1, W), F32)))
        db_ref[...] = acc

    return _pcall(body, name=name,
                  out_shape=[jax.ShapeDtypeStruct((S, W), F32), jax.ShapeDtypeStruct((1, W), F32)],
                  compiler_params=pltpu.CompilerParams(vmem_limit_bytes=VMEM_LIMIT))(dc, zf, bf)


def _proj_dil(h, w_qkv, *, name, tm=1024):
    S, D = h.shape
    tn = DIL_WIDTH

    def body(a_ref, b_ref, *rest):
        outs, acc = rest[:N_DIL_GROUPS], rest[N_DIL_GROUPS]
        prod = jnp.dot(a_ref[...].astype(_CD), b_ref[...].astype(_CD), preferred_element_type=F32)
        for k in range(tn // LANES):
            acc[k] = prod[:, k * LANES:(k + 1) * LANES]
        for g, (_, d) in enumerate(DIL_PAIRS):
            for half in range(DIL_OUT // LANES):
                k = g * (DIL_OUT // LANES) + half
                cols = slice(half * LANES, (half + 1) * LANES)
                for r in range(d):
                    rows = pl.ds(r, tm // d, stride=d) if d > 1 else slice(None)
                    outs[g][0, r, :, cols] = acc[k, rows, :].astype(outs[g].dtype)

    out_specs = [pl.BlockSpec((1, d, tm // d, DIL_OUT), lambda i, j: (j, 0, i, 0)) for _, d in DIL_PAIRS]
    out_shape = [jax.ShapeDtypeStruct((3, d, S // d, DIL_OUT), _CD) for _, d in DIL_PAIRS]
    outs = _pcall(body, name=name, grid=(S // tm, 3),
                  in_specs=[pl.BlockSpec((tm, D), lambda i, j: (i, 0)), pl.BlockSpec((D, tn), lambda i, j: (0, j))],
                  out_specs=out_specs, out_shape=out_shape, scratch_shapes=[pltpu.VMEM((tn // LANES, tm, LANES), F32)],
                  compiler_params=_params("parallel", "arbitrary"))(h, w_qkv)
    return [o.reshape(3, S, DIL_OUT) for o in outs]


def _dil_start(block, S, dilation):
    sub = S // dilation
    u0 = block * DIL_W
    return (u0 % sub) * dilation + u0 // sub


def _dil_slopes(group):
    h = np.arange(1, N_DIL_GROUPS * DIL_HEADS + 1, dtype=np.float32)
    s = (np.float32(2.0) ** (np.float32(-8.0) * h / np.float32(N_DIL_GROUPS * DIL_HEADS))).astype(np.float32)
    return [float(v) for v in s.reshape(N_DIL_GROUPS, DIL_HEADS)[group]]


def _dil_tiles(i, n, blocks_per_seq):
    qi = lax.broadcasted_iota(jnp.int32, (DIL_W, DIL_W), 0)
    kj = lax.broadcasted_iota(jnp.int32, (DIL_W, DIL_W), 1)
    first = ((4 * n + i) % blocks_per_seq) == 0
    valid_prev = jnp.logical_and(kj >= qi, jnp.logical_not(first))
    valid_cur = kj <= qi
    rel_prev = (qi - kj + DIL_W).astype(F32)
    rel_cur = (qi - kj).astype(F32)
    return valid_prev, valid_cur, rel_prev, rel_cur


CHUNK = 4 * DIL_W


def _dil_rows(block, S, dilation):
    start = _dil_start(block, S, dilation)
    return pl.ds(start, DIL_W, stride=dilation) if dilation > 1 else pl.ds(start, DIL_W)


def SPLIT(S):
    return (DIL_OUT // LANES, S, LANES)


def _dil_fwd(qkv, group, *, name):
    S = qkv.shape[1]
    dilation = DIL_PAIRS[group][1]
    bps = (S // dilation) // DIL_W
    slopes = _dil_slopes(group)
    nt = (((1,), (1,)), ((), ()))

    def body(q_ref, k_ref, v_ref, kp_ref, vp_ref, on_ref, ln_ref, o_ref, l_ref):
        n = pl.program_id(0)
        for i in range(4):
            valid_prev, valid_cur, rel_prev, rel_cur = _dil_tiles(i, n, bps)
            rows = slice(i * DIL_W, (i + 1) * DIL_W)
            prow = slice((i - 1) * DIL_W, i * DIL_W)
            for h in range(DIL_HEADS):
                cols = slice(h * HEAD_DIM, (h + 1) * HEAD_DIM)
                qh = q_ref[rows, cols]
                kc, vc = k_ref[rows, cols], v_ref[rows, cols]
                kp = kp_ref[:, cols] if i == 0 else k_ref[prow, cols]
                vp = vp_ref[:, cols] if i == 0 else v_ref[prow, cols]
                sl = slopes[h] * dilation
                sp = lax.dot_general(qh, kp, nt, preferred_element_type=F32) * ATTN_SCALE - sl * rel_prev
                sc = lax.dot_general(qh, kc, nt, preferred_element_type=F32) * ATTN_SCALE - sl * rel_cur
                sp = jnp.where(valid_prev, sp, NEG_INF)
                sc = jnp.where(valid_cur, sc, NEG_INF)
                m = jnp.maximum(jnp.max(sp, axis=-1, keepdims=True), jnp.max(sc, axis=-1, keepdims=True))
                pp, pc = jnp.exp(sp - m), jnp.exp(sc - m)
                den = jnp.sum(pp, axis=-1, keepdims=True) + jnp.sum(pc, axis=-1, keepdims=True)
                acc = (jnp.dot(pp.astype(_CD), vp, preferred_element_type=F32)
                       + jnp.dot(pc.astype(_CD), vc, preferred_element_type=F32))
                o_ref[rows, cols] = acc / den
                l_ref[rows, cols] = jnp.broadcast_to(m + jnp.log(den), (DIL_W, HEAD_DIM))
        for i in range(4):
            rows = slice(i * DIL_W, (i + 1) * DIL_W)
            nat = _dil_rows(4 * n + i, S, dilation)
            for half in range(DIL_OUT // LANES):
                cols = slice(half * LANES, (half + 1) * LANES)
                on_ref[half, nat, :] = o_ref[rows, cols]
                ln_ref[half, nat, :] = l_ref[rows, cols]

    def cur(which):
        return pl.BlockSpec((None, CHUNK, DIL_OUT), lambda n: (which, n, 0))

    def prev(which):
        return pl.BlockSpec((None, DIL_W, DIL_OUT), lambda n: (which, jnp.maximum(4 * n - 1, 0), 0))

    whole = pl.BlockSpec(SPLIT(S), lambda n: (0, 0, 0))
    return _pcall(body, name=name, grid=(S // CHUNK,), in_specs=[cur(0), cur(1), cur(2), prev(1), prev(2)],
                  out_specs=[whole, whole],
                  out_shape=[jax.ShapeDtypeStruct(SPLIT(S), F32), jax.ShapeDtypeStruct(SPLIT(S), F32)],
                  scratch_shapes=[pltpu.VMEM((CHUNK, DIL_OUT), F32), pltpu.VMEM((CHUNK, DIL_OUT), F32)],
                  compiler_params=_params("arbitrary"))(qkv, qkv, qkv, qkv, qkv)


def _dil_bwd(qkv, o, lse, do, dlse, group, *, name):
    S = qkv.shape[1]
    dilation = DIL_PAIRS[group][1]
    bps = (S // dilation) // DIL_W
    slopes = _dil_slopes(group)
    nchunk = S // CHUNK
    nt = (((1,), (1,)), ((), ()))
    tn = (((0,), (0,)), ((), ()))

    def body(q_ref, k_ref, v_ref, kp_ref, vp_ref, on_ref, ln_ref, don_ref, dln_ref, dq_ref, dk_ref, dv_ref,
             dk_s, dv_s, o_ref, l_ref, do_ref, dl_ref):
        step = pl.program_id(0)
        n = nchunk - 1 - step
        for i in range(4):
            rows = slice(i * DIL_W, (i + 1) * DIL_W)
            nat = _dil_rows(4 * n + i, S, dilation)
            for half in range(DIL_OUT // LANES):
                cols = slice(half * LANES, (half + 1) * LANES)
                o_ref[rows, cols] = on_ref[half, nat, :]
                l_ref[rows, cols] = ln_ref[half, nat, :]
                do_ref[rows, cols] = don_ref[half, nat, :]
                dl_ref[rows, cols] = dln_ref[half, nat, :]

        @pl.when(step == 0)
        def _():
            dk_s[CHUNK:, :] = jnp.zeros((DIL_W, DIL_OUT), F32)
            dv_s[CHUNK:, :] = jnp.zeros((DIL_W, DIL_OUT), F32)

        dk_s[:CHUNK, :] = jnp.zeros((CHUNK, DIL_OUT), F32)
        dv_s[:CHUNK, :] = jnp.zeros((CHUNK, DIL_OUT), F32)
        for i in range(4):
            valid_prev, valid_cur, rel_prev, rel_cur = _dil_tiles(i, n, bps)
            rows = slice(i * DIL_W, (i + 1) * DIL_W)
            prow = slice((i - 1) * DIL_W, i * DIL_W)
            s_prev = slice(i * DIL_W, (i + 1) * DIL_W)
            s_cur = slice((i + 1) * DIL_W, (i + 2) * DIL_W)
            for h in range(DIL_HEADS):
                cols = slice(h * HEAD_DIM, (h + 1) * HEAD_DIM)
                qh = q_ref[rows, cols]
                kc, vc = k_ref[rows, cols], v_ref[rows, cols]
                kp = kp_ref[:, cols] if i == 0 else k_ref[prow, cols]
                vp = vp_ref[:, cols] if i == 0 else v_ref[prow, cols]
                sl = slopes[h] * dilation
                lh = l_ref[rows, h * HEAD_DIM:h * HEAD_DIM + 1]
                sp = lax.dot_general(qh, kp, nt, preferred_element_type=F32) * ATTN_SCALE - sl * rel_prev
                sc = lax.dot_general(qh, kc, nt, preferred_element_type=F32) * ATTN_SCALE - sl * rel_cur
                pp = jnp.exp(jnp.where(valid_prev, sp, NEG_INF) - lh)
                pc = jnp.exp(jnp.where(valid_cur, sc, NEG_INF) - lh)
                doh = do_ref[rows, cols]
                dsum = jnp.sum(doh * o_ref[rows, cols], axis=-1, keepdims=True)
                shift = dl_ref[rows, h * HEAD_DIM:h * HEAD_DIM + 1] - dsum
                dob = doh.astype(_CD)
                dsp = pp * (lax.dot_general(dob, vp, nt, preferred_element_type=F32) + shift)
                dsc = pc * (lax.dot_general(dob, vc, nt, preferred_element_type=F32) + shift)
                dspb = (dsp * ATTN_SCALE).astype(_CD)
                dscb = (dsc * ATTN_SCALE).astype(_CD)
                dq_ref[rows, cols] = (jnp.dot(dspb, kp, preferred_element_type=F32)
                                      + jnp.dot(dscb, kc, preferred_element_type=F32)).astype(dq_ref.dtype)
                dk_s[s_prev, cols] += lax.dot_general(dspb, qh, tn, preferred_element_type=F32)
                dk_s[s_cur, cols] += lax.dot_general(dscb, qh, tn, preferred_element_type=F32)
                dv_s[s_prev, cols] += lax.dot_general(pp.astype(_CD), dob, tn, preferred_element_type=F32)
                dv_s[s_cur, cols] += lax.dot_general(pc.astype(_CD), dob, tn, preferred_element_type=F32)
        dk_ref[...] = dk_s[DIL_W:, :].astype(dk_ref.dtype)
        dv_ref[...] = dv_s[DIL_W:, :].astype(dv_ref.dtype)
        dk_s[CHUNK:, :] = dk_s[:DIL_W, :]
        dv_s[CHUNK:, :] = dv_s[:DIL_W, :]

    def cur(which):
        return pl.BlockSpec((None, CHUNK, DIL_OUT), lambda s: (which, nchunk - 1 - s, 0))

    def prev(which):
        return pl.BlockSpec((None, DIL_W, DIL_OUT), lambda s: (which, jnp.maximum(4 * (nchunk - 1 - s) - 1, 0), 0))

    whole = pl.BlockSpec(SPLIT(S), lambda s: (0, 0, 0))
    out = pl.BlockSpec((CHUNK, DIL_OUT), lambda s: (nchunk - 1 - s, 0))
    shp = jax.ShapeDtypeStruct((S, DIL_OUT), _CD)
    tile = pltpu.VMEM((CHUNK, DIL_OUT), F32)
    return _pcall(body, name=name, grid=(nchunk,),
                  in_specs=[cur(0), cur(1), cur(2), prev(1), prev(2), whole, whole, whole, whole],
                  out_specs=[out, out, out], out_shape=[shp, shp, shp],
                  scratch_shapes=[pltpu.VMEM((CHUNK + DIL_W, DIL_OUT), F32), pltpu.VMEM((CHUNK + DIL_W, DIL_OUT), F32),
                                  tile, tile, tile, tile],
                  compiler_params=_params("arbitrary"))(qkv, qkv, qkv, qkv, qkv, o, lse, do, dlse)


def _dil_mix_fwd(os_, ls_, *, name, tm=512):
    nh, S, _ = os_[0].shape

    def body(o0, o1, o2, l0, l1, l2, out_ref):
        for half in range(nh):
            ls = [l0[half], l1[half], l2[half]]
            m = jnp.maximum(jnp.maximum(ls[0], ls[1]), ls[2])
            es = [jnp.exp(l - m) for l in ls]
            den = es[0] + es[1] + es[2]
            mixed = (es[0] * o0[half] + es[1] * o1[half] + es[2] * o2[half]) / den
            out_ref[:, half * LANES:(half + 1) * LANES] = mixed.astype(out_ref.dtype)

    halves = pl.BlockSpec((nh, tm, LANES), lambda i: (0, i, 0))
    row = pl.BlockSpec((tm, nh * LANES), lambda i: (i, 0))
    return _pcall(body, name=name, grid=(S // tm,), in_specs=[halves] * 6, out_specs=row,
                  out_shape=jax.ShapeDtypeStruct((S, nh * LANES), _CD), compiler_params=_params("parallel"))(*os_, *ls_)


def _dil_mix_bwd(doa, os_, ls_, *, name, tm=512, after=None):
    nh, S, _ = os_[0].shape

    def body(d_ref, o0, o1, o2, l0, l1, l2, do0, do1, do2, dl0, dl1, dl2):
        for half in range(nh):
            dv = d_ref[:, half * LANES:(half + 1) * LANES]
            ls = [l0[half], l1[half], l2[half]]
            m = jnp.maximum(jnp.maximum(ls[0], ls[1]), ls[2])
            es = [jnp.exp(l - m) for l in ls]
            den = es[0] + es[1] + es[2]
            al = [e / den for e in es]
            da = [_head_sum(dv * o[half]) for o in (o0, o1, o2)]
            mean = al[0] * da[0] + al[1] * da[1] + al[2] * da[2]
            for a, d_, do_ref, dl_ref in zip(al, da, (do0, do1, do2), (dl0, dl1, dl2)):
                do_ref[half] = a * dv
                dl_ref[half] = a * (d_ - mean)

    halves = pl.BlockSpec((nh, tm, LANES), lambda i: (0, i, 0))
    row = pl.BlockSpec((tm, nh * LANES), lambda i: (i, 0))
    shp = jax.ShapeDtypeStruct((nh, S, LANES), F32)
    return _pcall(body, after, name=name, grid=(S // tm,), in_specs=[row] + [halves] * 6, out_specs=[halves] * 6,
                  out_shape=[shp] * 6, compiler_params=_params("parallel"))(doa, *os_, *ls_)


FOX_T = 512


PACK = 2 * HEAD_DIM
HEAD_PAIRS = N_FOX_HEADS // 2
FOX_HPS = 8
Q_BLOCK0 = 0
K_BLOCK0 = FOX_WIDTH // PACK
V_BLOCK0 = 2 * FOX_WIDTH // PACK


def _pieces(x):
    hi = x.astype(jnp.bfloat16).astype(F32)
    r = x - hi
    mid = r.astype(jnp.bfloat16).astype(F32)
    lo = (r - mid).astype(jnp.bfloat16).astype(F32)
    return [hi, mid, lo]


def _extras(first, second, rows):
    lane = lax.broadcasted_iota(jnp.int32, (rows, HEAD_DIM), 1)
    out = jnp.zeros((rows, HEAD_DIM), F32)
    for idx, val in enumerate(list(first) + list(second)):
        out = jnp.where(lane == idx, val, out)
    return out


def _head_column(c, h):
    lane = lax.broadcasted_iota(jnp.int32, c.shape, 1)
    return jnp.sum(jnp.where(lane == h, c, 0.0), axis=1, keepdims=True)


ONES3 = [1.0, 1.0, 1.0]
ZEROS3 = [0.0, 0.0, 0.0]


def _fox_pack_fwd(qkv, c, *, name, tm=512):
    S = qkv.shape[0]

    def body(q_ref, k_ref, v_ref, c_ref, qo_ref, ko_ref, vo_ref):
        hp = pl.program_id(1)
        cv = c_ref[...]
        for hh in range(2):
            ch = _pieces(_head_column(cv, 2 * hp + hh))
            src = slice(hh * HEAD_DIM, (hh + 1) * HEAD_DIM)
            lo = slice(hh * PACK, hh * PACK + HEAD_DIM)
            hi = slice(hh * PACK + HEAD_DIM, (hh + 1) * PACK)
            qo_ref[:, lo] = (q_ref[:, src].astype(F32) * ATTN_SCALE).astype(qo_ref.dtype)
            qo_ref[:, hi] = _extras(ch, ONES3, tm).astype(qo_ref.dtype)
            ko_ref[:, lo] = k_ref[:, src]
            ko_ref[:, hi] = _extras(ONES3, [-p for p in ch], tm).astype(ko_ref.dtype)
            vo_ref[:, lo] = v_ref[:, src]
            vo_ref[:, hi] = _extras(ONES3, ZEROS3, tm).astype(vo_ref.dtype)

    def src(block0):
        return pl.BlockSpec((tm, PACK), lambda i, hp: (i, block0 + hp))

    out = pl.BlockSpec((tm, 2 * PACK), lambda i, hp: (i, hp))
    shp = jax.ShapeDtypeStruct((S, N_FOX_HEADS * PACK), _CD)
    return _pcall(body, name=name, grid=(S // tm, HEAD_PAIRS),
                  in_specs=[src(Q_BLOCK0), src(K_BLOCK0), src(V_BLOCK0), pl.BlockSpec((tm, PACK), lambda i, hp: (i, 0))],
                  out_specs=[out, out, out], out_shape=[shp, shp, shp],
                  compiler_params=_params("parallel", "parallel"))(qkv, qkv, qkv, c)


def _fox_fwd(qp, kp, vp, *, name):
    S = qp.shape[0]
    nt = S // FOX_T
    nt_dims = (((1,), (1,)), ((), ()))
    tn_dims = (((0,), (0,)), ((), ()))

    def body(i_tab, j_tab, q_ref, k_ref, v_ref, o_ref, l_ref, m_s, acc_s):
        t = pl.program_id(1)
        i, j = i_tab[t], j_tab[t]

        @pl.when(j == 0)
        def _():
            m_s[...] = jnp.full((FOX_HPS, 1, FOX_T), NEG_INF, F32)
            acc_s[...] = jnp.zeros((FOX_HPS, PACK, FOX_T), F32)

        def tile(diagonal):
            for hh in range(FOX_HPS):
                cols = slice(hh * PACK, (hh + 1) * PACK)
                st = lax.dot_general(k_ref[:, cols], q_ref[:, cols], nt_dims, preferred_element_type=F32)
                if diagonal:
                    key = lax.broadcasted_iota(jnp.int32, (FOX_T, FOX_T), 0)
                    qry = lax.broadcasted_iota(jnp.int32, (FOX_T, FOX_T), 1)
                    st = jnp.where(key <= qry, st, NEG_INF)
                m_old = m_s[hh]
                m_new = jnp.maximum(m_old, jnp.max(st, axis=0, keepdims=True))
                pt = jnp.exp(st - m_new)
                acc_s[hh] = jnp.exp(m_old - m_new) * acc_s[hh] + lax.dot_general(
                    v_ref[:, cols], pt.astype(_CD), tn_dims, preferred_element_type=F32)
                m_s[hh] = m_new

        @pl.when(j < i)
        def _():
            tile(False)

        @pl.when(j == i)
        def _():
            tile(True)
            for hh in range(FOX_HPS):
                acc = acc_s[hh]
                den = acc[HEAD_DIM:HEAD_DIM + 1, :]
                cols = slice(hh * HEAD_DIM, (hh + 1) * HEAD_DIM)
                o_ref[:, cols] = (acc[:HEAD_DIM, :] / den).T
                l_ref[:, cols] = jnp.broadcast_to(m_s[hh] + jnp.log(den), (HEAD_DIM, FOX_T)).T

    pairs = [(i, j) for i in range(nt) for j in range(i + 1)]
    i_tab = jnp.asarray([p[0] for p in pairs], jnp.int32)
    j_tab = jnp.asarray([p[1] for p in pairs], jnp.int32)
    qs = pl.BlockSpec((FOX_T, FOX_HPS * PACK), lambda hp, t, it, jt: (it[t], hp))
    ks = pl.BlockSpec((FOX_T, FOX_HPS * PACK), lambda hp, t, it, jt: (jt[t], hp))
    os_ = pl.BlockSpec((FOX_T, FOX_HPS * HEAD_DIM), lambda hp, t, it, jt: (it[t], hp))
    shp = jax.ShapeDtypeStruct((S, FOX_WIDTH), F32)
    grid_spec = pltpu.PrefetchScalarGridSpec(
        num_scalar_prefetch=2, grid=(N_FOX_HEADS // FOX_HPS, len(pairs)), in_specs=[qs, ks, ks], out_specs=[os_, os_],
        scratch_shapes=[pltpu.VMEM((FOX_HPS, 1, FOX_T), F32), pltpu.VMEM((FOX_HPS, PACK, FOX_T), F32)])
    return _pcall(body, name=name, grid_spec=grid_spec, out_shape=[shp, shp],
                  compiler_params=_params("parallel", "arbitrary"))(i_tab, j_tab, qp, kp, vp)


def _fox_pack_bwd(qkv, c, o, lse, do, *, name, tm=512, after=None):
    S = qkv.shape[0]

    def body(q_ref, c_ref, o_ref, l_ref, do_ref, qo_ref, do_out_ref):
        hp = pl.program_id(1)
        cv = c_ref[...]
        for hh in range(2):
            src = slice(hh * HEAD_DIM, (hh + 1) * HEAD_DIM)
            lo = slice(hh * PACK, hh * PACK + HEAD_DIM)
            hi = slice(hh * PACK + HEAD_DIM, (hh + 1) * PACK)
            shift = _head_column(cv, 2 * hp + hh) - l_ref[:, hh * HEAD_DIM:hh * HEAD_DIM + 1]
            dov = do_ref[:, src]
            dsum = jnp.sum(dov * o_ref[:, src], axis=-1, keepdims=True)
            qo_ref[:, lo] = (q_ref[:, src].astype(F32) * ATTN_SCALE).astype(qo_ref.dtype)
            qo_ref[:, hi] = _extras(_pieces(shift), ONES3, tm).astype(qo_ref.dtype)
            do_out_ref[:, lo] = dov.astype(do_out_ref.dtype)
            do_out_ref[:, hi] = _extras(_pieces(-dsum), ZEROS3, tm).astype(do_out_ref.dtype)

    pair = pl.BlockSpec((tm, PACK), lambda i, hp: (i, hp))
    out = pl.BlockSpec((tm, 2 * PACK), lambda i, hp: (i, hp))
    shp = jax.ShapeDtypeStruct((S, N_FOX_HEADS * PACK), _CD)
    return _pcall(body, after, name=name, grid=(S // tm, HEAD_PAIRS),
                  in_specs=[pl.BlockSpec((tm, PACK), lambda i, hp: (i, Q_BLOCK0 + hp)),
                            pl.BlockSpec((tm, PACK), lambda i, hp: (i, 0)), pair, pair, pair],
                  out_specs=[out, out], out_shape=[shp, shp],
                  compiler_params=_params("parallel", "parallel"))(qkv, c, o, lse, do)


def _fox_bwd(qp, kp, vp, dop, *, name):
    S = qp.shape[0]
    nt = S // FOX_T
    nt_dims = (((1,), (1,)), ((), ()))
    tn_dims = (((0,), (0,)), ((), ()))

    def body(i_tab, j_tab, q_ref, k_ref, v_ref, do_ref, dq_ref, dk_ref, dv_ref, dc_ref, dr_ref,
             dq_s, dk_s, dv_s, dc_s, dr_s):
        t = pl.program_id(1)
        i, j = i_tab[t], j_tab[t]

        @pl.when(t == 0)
        def _():
            dq_s[...] = jnp.zeros((S, FOX_HPS * PACK), F32)
            dr_s[...] = jnp.zeros((FOX_HPS, 1, S), F32)

        @pl.when(i == j)
        def _():
            dk_s[...] = jnp.zeros((FOX_T, FOX_HPS * PACK), F32)
            dv_s[...] = jnp.zeros((FOX_T, FOX_HPS * PACK), F32)
            dc_s[...] = jnp.zeros((FOX_HPS, FOX_T, 1), F32)

        def tile(diagonal):
            rows = pl.ds(pl.multiple_of(i * FOX_T, FOX_T), FOX_T)
            for hh in range(FOX_HPS):
                cols = slice(hh * PACK, (hh + 1) * PACK)
                qv, kv, vv, dov = q_ref[:, cols], k_ref[:, cols], v_ref[:, cols], do_ref[:, cols]
                pt = jnp.exp(lax.dot_general(kv, qv, nt_dims, preferred_element_type=F32))
                if diagonal:
                    key = lax.broadcasted_iota(jnp.int32, (FOX_T, FOX_T), 0)
                    qry = lax.broadcasted_iota(jnp.int32, (FOX_T, FOX_T), 1)
                    pt = jnp.where(key <= qry, pt, 0.0)
                dst = pt * lax.dot_general(vv, dov, nt_dims, preferred_element_type=F32)
                dsb = dst.astype(_CD)
                dc_s[hh] += jnp.sum(dst, axis=1, keepdims=True)
                dr_s[hh, :, rows] += jnp.sum(dst, axis=0, keepdims=True)
                dv_s[:, cols] += jnp.dot(pt.astype(_CD), dov, preferred_element_type=F32)
                dk_s[:, cols] += jnp.dot(dsb, qv, preferred_element_type=F32)
                dq_s[rows, cols] += lax.dot_general(dsb, kv, tn_dims, preferred_element_type=F32)

        @pl.when(i > j)
        def _():
            tile(False)

        @pl.when(i == j)
        def _():
            tile(True)

        @pl.when(i == nt - 1)
        def _():
            for hh in range(FOX_HPS):
                src = slice(hh * PACK, hh * PACK + HEAD_DIM)
                dst_cols = slice(hh * HEAD_DIM, (hh + 1) * HEAD_DIM)
                dk_ref[:, dst_cols] = dk_s[:, src].astype(dk_ref.dtype)
                dv_ref[:, dst_cols] = dv_s[:, src].astype(dv_ref.dtype)
                dc_ref[:, dst_cols] = jnp.broadcast_to(dc_s[hh], (FOX_T, HEAD_DIM))

        @pl.when(t == len(pairs) - 1)
        def _():
            for hh in range(FOX_HPS):
                dq_ref[:, hh * HEAD_DIM:(hh + 1) * HEAD_DIM] = (
                    dq_s[:, hh * PACK:hh * PACK + HEAD_DIM] * ATTN_SCALE).astype(dq_ref.dtype)
            dr_ref[...] = dr_s[...]

    pairs = [(i, j) for j in range(nt) for i in range(j, nt)]
    i_tab = jnp.asarray([p[0] for p in pairs], jnp.int32)
    j_tab = jnp.asarray([p[1] for p in pairs], jnp.int32)
    wide, narrow = FOX_HPS * PACK, FOX_HPS * HEAD_DIM
    qs = pl.BlockSpec((FOX_T, wide), lambda hp, t, it, jt: (it[t], hp))
    ks = pl.BlockSpec((FOX_T, wide), lambda hp, t, it, jt: (jt[t], hp))
    whole = pl.BlockSpec((S, narrow), lambda hp, t, it, jt: (0, hp))
    cs = pl.BlockSpec((FOX_T, narrow), lambda hp, t, it, jt: (jt[t], hp))
    rs = pl.BlockSpec((FOX_HPS, 1, S), lambda hp, t, it, jt: (hp, 0, 0))
    shp = jax.ShapeDtypeStruct((S, FOX_WIDTH), _CD)
    grid_spec = pltpu.PrefetchScalarGridSpec(
        num_scalar_prefetch=2, grid=(N_FOX_HEADS // FOX_HPS, len(pairs)), in_specs=[qs, ks, ks, qs],
        out_specs=[whole, cs, cs, cs, rs],
        scratch_shapes=[pltpu.VMEM((S, wide), F32), pltpu.VMEM((FOX_T, wide), F32),
                        pltpu.VMEM((FOX_T, wide), F32), pltpu.VMEM((FOX_HPS, FOX_T, 1), F32),
                        pltpu.VMEM((FOX_HPS, 1, S), F32)])
    return _pcall(body, name=name, grid_spec=grid_spec,
                  out_shape=[shp, shp, shp, jax.ShapeDtypeStruct((S, FOX_WIDTH), F32),
                             jax.ShapeDtypeStruct((N_FOX_HEADS, 1, S), F32)],
                  compiler_params=_params("parallel", "arbitrary"))(i_tab, j_tab, qp, kp, vp, dop)


def _redilate(t, d):
    if d == 1:
        return t
    S, C = t.shape
    return t.reshape(d, S // d, C).transpose(1, 0, 2).reshape(S, C)


def _layer_step(x, tgt, w, p, late_weights=None, grad_sink=None, after=None, first_weights=None):
    S = x.shape[0]
    after_norm, after_proj = after if after is not None else (None, None)
    h = _rms_fwd(x, p["norm_mix_g"], name="rms_mix", after=after_norm)
    if first_weights is not None:
        w = {**w, **first_weights(h)}
    qkv = _mm(h, w["qkv"][:, 3 * DIL_WIDTH:], name="proj_fox", out_dtype=_CD, tn=768, tm=2048, after=after_proj)
    dil_qkv = _proj_dil(h, w["qkv"], name="proj_dil")
    zf = _mm(h, w["f"], name="proj_f")
    gl = _mm(h, w["g"], name="proj_gate", tn=1024, out_dtype=_CD)

    dil_o, dil_l = [], []
    for g in range(N_DIL_GROUPS):
        og, lg = _dil_fwd(dil_qkv[g], g, name=f"dil_fwd{g}")
        dil_o.append(og), dil_l.append(lg)
    o_a = _dil_mix_fwd(dil_o, dil_l, name="dil_mix")

    c = _fox_cumsum(zf, p["b_fgt"], name="fox_cumsum")
    fqp, fkp, fvp = _fox_pack_fwd(qkv, c, name="fox_pack")
    o_b, flse = _fox_fwd(fqp, fkp, fvp, name="fox_fwd")

    if late_weights is not None:
        w = {**w, **late_weights(o_b)}
    y_a = _mm(o_a, w["dil_out"], name="y_a", tn=1024, out_dtype=_CD)
    y_b = _mm(o_b, w["fox_out"], name="y_b", tn=1024, out_dtype=_CD)
    merged = _gate_fwd(gl, p["b_gate"], y_a, y_b, name="gate_fwd")
    x1 = _mm(merged, w["out"], name="mix_out", add=x)

    h2 = _rms_fwd(x1, p["norm_ffn_g"], name="rms_ffn")
    gate, up, act = _ffn_in_act(h2, w["ffn_in"], name="ffn_in")
    x2 = _mm(act, w["ffn_down"], name="ffn_down", add=x1, tk=2816)

    loss, dx2, dg_final = _loss_head(x2, p["norm_final_g"], tgt, name="loss_head")

    gw_ffn_down = _mm(act, dx2, name="gw_ffn_down", ta=True, out_dtype=_CD, tm=1408)
    dgu = _d_swiglu(dx2, w["ffn_down"], gate, up, name="d_swiglu")
    dh2 = _mm(dgu, w["ffn_in"], name="d_h2", tb=True, tk=1408, b_blocks=True, tm=2048, a_halves=True)
    gw_ffn_in = _mm(h2, dgu, name="gw_ffn_in", ta=True, out_dtype=_CD, tn=1408, out_blocks=1408, b_halves=True)
    sink = grad_sink if grad_sink is not None else (lambda group, grads: None)
    tok = sink("ffn", dict(ffn_in=gw_ffn_in, ffn_down=gw_ffn_down))
    dx1, dg_ffn = _rms_bwd(x1, p["norm_ffn_g"], dh2, dx2, name="rms_ffn_bwd", after=tok)

    dmerged = _mm(dx1, w["out"], name="d_merged", tb=True, out_dtype=_CD)
    gw_out = _mm(merged, dx1, name="gw_out", ta=True, out_dtype=_CD)
    dy_a, dy_b, dgl, db_gate = _gate_bwd(dmerged, gl, p["b_gate"], y_a, y_b, name="gate_bwd")
    do_a = _mm(dy_a, w["dil_out"], name="d_o_a", tb=True)
    gw_dil_out = _mm(o_a, dy_a, name="gw_dil_out", ta=True, out_dtype=_CD, tn=1024)
    do_b = _mm(dy_b, w["fox_out"], name="d_o_b", tb=True)
    gw_fox_out = _mm(o_b, dy_b, name="gw_fox_out", ta=True, out_dtype=_CD, tn=1024)
    tok = sink("mix", dict(dil_out=gw_dil_out, fox_out=gw_fox_out, out=gw_out))

    bqp, bdop = _fox_pack_bwd(qkv, c, o_b, flse, do_b, name="fox_pack_bwd", after=tok)
    dqp, dkp, dvp, dck, dcq = _fox_bwd(bqp, fkp, fvp, bdop, name="fox_bwd")
    dc = dcq[:, 0, :].T - dck.reshape(S, N_FOX_HEADS, HEAD_DIM)[:, :, 0]
    dc = jnp.pad(dc, ((0, 0), (0, F_PAD - N_FOX_HEADS)))
    dzf, db_fgt = _fox_cumsum_bwd(dc, zf, p["b_fgt"], name="fox_cumsum_bwd")

    douts = _dil_mix_bwd(do_a, dil_o, dil_l, name="dil_mix_bwd", after=tok)
    dqs, dks, dvs = [], [], []
    for g, (_, d) in enumerate(DIL_PAIRS):
        dq, dk, dv = _dil_bwd(dil_qkv[g], dil_o[g], dil_l[g], douts[g], douts[3 + g], g, name=f"dil_bwd{g}")
        dqs.append(_redilate(dq, d)), dks.append(_redilate(dk, d)), dvs.append(_redilate(dv, d))
    dqkv = jnp.concatenate(dqs + dks + dvs + [dqp, dkp, dvp], axis=1)

    gw_qkv = _mm(h, dqkv, name="gw_qkv", ta=True, out_dtype=_CD, tn=768)
    gw_g = _mm(h, dgl, name="gw_gate", ta=True, out_dtype=_CD)
    gw_f = _mm(h, dzf, name="gw_f", ta=True, out_dtype=_CD)
    tok = sink("in", dict(qkv=gw_qkv, f=gw_f, g=gw_g))
    dh = _mm(dqkv, w["qkv"], name="d_h_qkv", tb=True, tk=1920, tm=2048, after=tok)
    dh = _mm(dgl, w["g"], name="d_h_gate", tb=True, add=dh)
    dh = _mm(dzf, w["f"], name="d_h_f", tb=True, add=dh)
    dx, dg_mix = _rms_bwd(x, p["norm_mix_g"], dh, dx1, name="rms_mix_bwd")

    gw = dict(qkv=gw_qkv, f=gw_f, g=gw_g, dil_out=gw_dil_out, fox_out=gw_fox_out, out=gw_out, ffn_in=gw_ffn_in,
              ffn_down=gw_ffn_down)
    small = dict(norm_mix_g=dg_mix, b_fgt=db_fgt, b_gate=db_gate, norm_ffn_g=dg_ffn, norm_final_g=dg_final)
    return loss, dx, gw, small


def _position():
    return lax.axis_index("x"), lax.axis_index("y"), lax.axis_index("c")


def _other_chips(x, y):
    return [(1 - x, y), (x, 1 - y), (1 - x, 1 - y)]


ROW_TILE = 16


def _row_chunks(rows, want=4):
    n = want
    while n > 1 and rows % (n * ROW_TILE):
        n //= 2
    return n


SEM_SPEC = pl.BlockSpec(memory_space=pltpu.SEMAPHORE)
ANY_SPEC = pl.BlockSpec(memory_space=pl.ANY)
DATAFLOW = pltpu.SideEffectType.DATAFLOW_SIDE_EFFECTING


def _in_hbm(a):
    return pltpu.with_memory_space_constraint(a, pltpu.HBM)


def _split_copy_start(srcs, land_shapes, copies, after, *, name):
    n, m = len(srcs), len(land_shapes)

    def body(*refs):
        src_refs, land_refs = refs[:n], refs[n:n + m]
        send_sems, recv_sems = refs[n + m + 1], refs[n + m + 2]
        token = refs[-1]
        x, y, c = _position()
        for k, (src, dst, peer) in enumerate(copies(x, y, c, src_refs, land_refs)):
            pltpu.make_async_remote_copy(src_ref=src, dst_ref=dst, send_sem=send_sems.at[k], recv_sem=recv_sems.at[k],
                                         device_id=peer, device_id_type=MESH).start()
        token[...] = jnp.zeros_like(token)

    lands = [lax.empty(s.shape, s.dtype) for s in land_shapes]
    count = len(copies(0, 0, 0, srcs, lands))
    out = _pcall(
        body, name=name,
        out_shape=(pltpu.SemaphoreType.DMA((count,)), pltpu.SemaphoreType.DMA((count,)),
                   *[pltpu.HBM(s.shape, s.dtype) for s in srcs], *[pltpu.HBM(s.shape, s.dtype) for s in land_shapes],
                   jax.ShapeDtypeStruct((8, 128), F32)),
        in_specs=[HBM_SPEC] * (n + m) + [ANY_SPEC],
        out_specs=(SEM_SPEC, SEM_SPEC, *[HBM_SPEC] * (n + m), pl.BlockSpec(memory_space=pltpu.VMEM)),
        input_output_aliases={k: 2 + k for k in range(n + m)},
        compiler_params=pltpu.CompilerParams(has_side_effects=DATAFLOW),
    )(*[_in_hbm(s) for s in srcs], *[_in_hbm(l) for l in lands], after)
    return out[0], out[1], list(out[2:2 + n]), list(out[2 + n:2 + n + m]), out[-1]


def _split_copy_wait(send_sems, recv_sems, srcs, lands, copies, after, *, name):
    n, m = len(srcs), len(lands)

    def body(*refs):
        src_refs, land_refs = refs[:n], refs[n:n + m]
        send, recv = refs[n + m], refs[n + m + 1]
        x, y, c = _position()
        for k, (src, dst, peer) in enumerate(copies(x, y, c, src_refs, land_refs)):
            cp = pltpu.make_async_remote_copy(src_ref=src, dst_ref=dst, send_sem=send.at[k], recv_sem=recv.at[k],
                                              device_id=peer, device_id_type=MESH)
            cp.wait_send()
            cp.wait_recv()

    afters = list(after) if isinstance(after, (list, tuple)) else [after]
    out = _pcall(
        body, name=name,
        out_shape=tuple(pltpu.HBM(s.shape, s.dtype) for s in list(srcs) + list(lands)),
        in_specs=[HBM_SPEC] * (n + m) + [SEM_SPEC, SEM_SPEC] + [ANY_SPEC] * len(afters),
        out_specs=tuple([HBM_SPEC] * (n + m)),
        input_output_aliases={k: k for k in range(n + m)},
        compiler_params=pltpu.CompilerParams(has_side_effects=DATAFLOW),
    )(*srcs, *lands, send_sems, recv_sems, *afters)
    return list(out[:n]), list(out[n:])


def _gather_copies(x, y, c, shard_refs, land_refs):
    out = []
    for s, l in zip(shard_refs, land_refs):
        half = s.shape[0] // 2
        nq = _row_chunks(half)
        for cx, cy in _other_chips(x, y):
            for q in range(nq):
                rows = pl.ds(c * half + q * (half // nq), half // nq)
                out.append((s.at[rows, :], l.at[2 * x + y, rows, :], (cx, cy, c)))
    return out


def _scatter_copies(x, y, c, part_refs, land_refs):
    out = []
    for p, l in zip(part_refs, land_refs):
        nq = _row_chunks(p.shape[1])
        for r, (cx, cy) in enumerate(_other_chips(x, y)):
            for q in range(nq):
                rows = pl.ds(q * (p.shape[1] // nq), p.shape[1] // nq)
                out.append((p.at[2 * cx + cy, rows, :], l.at[r, rows, :], (cx, cy, c)))
    return out


def _forward_halves(lands, *, name):
    n = len(lands)

    def body(*refs):
        ins = refs[:n]
        send_sems, recv_sems = refs[2 * n:]
        x, y, c = _position()
        copies = []
        for w in range(n):
            half = ins[w].shape[1] // 2
            for r, (cx, cy) in enumerate(_other_chips(x, y)):
                blk = ins[w].at[2 * cx + cy, pl.ds(c * half, half), :]
                cp = pltpu.make_async_remote_copy(src_ref=blk, dst_ref=blk, send_sem=send_sems.at[w, r],
                                                  recv_sem=recv_sems.at[w, r], device_id=(x, y, 1 - c),
                                                  device_id_type=MESH)
                cp.start()
                copies.append(cp)
        for w in range(n):
            half = ins[w].shape[1] // 2
            for r, (cx, cy) in enumerate(_other_chips(x, y)):
                blk = ins[w].at[2 * cx + cy, pl.ds((1 - c) * half, half), :]
                pltpu.make_async_remote_copy(src_ref=blk, dst_ref=blk, send_sem=send_sems.at[w, r],
                                             recv_sem=recv_sems.at[w, r], device_id=(x, y, 1 - c),
                                             device_id_type=MESH).wait_recv()
        for cp in copies:
            cp.wait_send()

    return _pcall(
        body, name=name, in_specs=[HBM_SPEC] * n, out_specs=[HBM_SPEC] * n,
        out_shape=[jax.ShapeDtypeStruct(l.shape, l.dtype) for l in lands],
        input_output_aliases={k: k for k in range(n)},
        scratch_shapes=[pltpu.SemaphoreType.DMA((n, 3)), pltpu.SemaphoreType.DMA((n, 3))],
    )(*lands)


def _swap_halves(grads, name="swap_halves"):
    n = len(grads)

    def body(*refs):
        ins, outs = refs[:n], refs[n:2 * n]
        send_sems, recv_sems = refs[2 * n:]
        x, y, c = _position()
        copies = []
        for w in range(n):
            half = ins[w].shape[1] // 2
            cp = pltpu.make_async_remote_copy(
                src_ref=ins[w].at[:, pl.ds((1 - c) * half, half), :], dst_ref=outs[w], send_sem=send_sems.at[w],
                recv_sem=recv_sems.at[w], device_id=(x, y, 1 - c), device_id_type=MESH)
            cp.start()
            copies.append(cp)
        for cp in copies:
            cp.wait()

    return _pcall(
        body, name=name, in_specs=[HBM_SPEC] * n, out_specs=[HBM_SPEC] * n,
        out_shape=[jax.ShapeDtypeStruct((4, g.shape[1] // 2, g.shape[2]), g.dtype) for g in grads],
        scratch_shapes=[pltpu.SemaphoreType.DMA((n,)), pltpu.SemaphoreType.DMA((n,))],
    )(*grads)


def _share_halves(halves):
    n = len(halves)

    def body(*refs):
        ins, outs = refs[:n], refs[n:2 * n]
        send_sems, recv_sems = refs[2 * n:]
        x, y, c = _position()
        copies = []
        for w in range(n):
            cp = pltpu.make_async_remote_copy(src_ref=ins[w], dst_ref=outs[w], send_sem=send_sems.at[w],
                                              recv_sem=recv_sems.at[w], device_id=(x, y, 1 - c), device_id_type=MESH)
            cp.start()
            copies.append(cp)
        for cp in copies:
            cp.wait()

    return _pcall(
        body, name="share_halves", in_specs=[HBM_SPEC] * n, out_specs=[HBM_SPEC] * n,
        out_shape=[jax.ShapeDtypeStruct(h.shape, h.dtype) for h in halves],
        scratch_shapes=[pltpu.SemaphoreType.DMA((n,)), pltpu.SemaphoreType.DMA((n,))],
    )(*halves)


def _sum_small(part):
    rows, width = part.shape

    def body(x_ref, out_ref, all_ref, send_sems, recv_sems):
        x, y, c = _position()
        me, sibling = (x, y, c), (x, y, 1 - c)
        chips = _other_chips(x, y)

        def block(px, py, pc):
            return all_ref.at[pl.ds((4 * px + 2 * py + pc) * rows, rows), :]

        def copy(k, blk, to, src=None):
            return pltpu.make_async_remote_copy(
                src_ref=block(*blk) if src is None else src, dst_ref=block(*blk), send_sem=send_sems.at[k],
                recv_sem=recv_sems.at[k], device_id=to, device_id_type=MESH)

        all_ref[pl.ds((4 * x + 2 * y + c) * rows, rows), :] = x_ref[...]
        first = [copy(0, me, sibling, src=x_ref)]
        first += [copy(1 + j, me, (*chip, c), src=x_ref) for j, chip in enumerate(chips)]
        for cp in first:
            cp.start()
        passed = [copy(4 + j, (*chip, c), sibling) for j, chip in enumerate(chips)]
        for j, chip in enumerate(chips):
            copy(1 + j, (*chip, c), me).wait_recv()
            passed[j].start()
        copy(0, sibling, me).wait_recv()
        for j, chip in enumerate(chips):
            copy(4 + j, (*chip, 1 - c), me).wait_recv()
        for cp in first + passed:
            cp.wait_send()
        total = all_ref[0:rows, :]
        for d in range(1, 8):
            total = total + all_ref[d * rows:(d + 1) * rows, :]
        out_ref[...] = total

    vm = pl.BlockSpec(memory_space=pltpu.VMEM)
    return _pcall(
        body, name="sum_small", in_specs=[vm], out_specs=vm, out_shape=jax.ShapeDtypeStruct((rows, width), F32),
        scratch_shapes=[pltpu.VMEM((8 * rows, width), F32), pltpu.SemaphoreType.DMA((7,)), pltpu.SemaphoreType.DMA((7,))],
    )(part)


def _row_tile(R, C, itemsize=4, budget=1 << 20):
    for t in (512, 256, 128, 64, 32, 16, 8):
        if R % t == 0 and t * C * itemsize <= budget:
            return t
    return R


def _add_halves(g, recv, c, *, name):
    _, R, C = g.shape
    half = R // 2
    t = _row_tile(half, C)
    nb = half // t

    def body(c_ref, g_ref, r_ref, o_ref):
        o_ref[...] = (g_ref[...].astype(F32) + r_ref[...].astype(F32)).astype(o_ref.dtype)

    grid_spec = pltpu.PrefetchScalarGridSpec(
        num_scalar_prefetch=1, grid=(4, nb),
        in_specs=[pl.BlockSpec((1, t, C), lambda k, i, cr: (k, cr[0] * nb + i, 0)),
                  pl.BlockSpec((1, t, C), lambda k, i, cr: (k, i, 0))],
        out_specs=pl.BlockSpec((1, t, C), lambda k, i, cr: (k, i, 0)))
    return _pcall(body, name=name, grid_spec=grid_spec, out_shape=jax.ShapeDtypeStruct((4, half, C), g.dtype),
                  compiler_params=_params("parallel", "parallel"))(c, g, recv)


def _add_owners(mine, recv, *, name):
    half, C = mine.shape
    t = _row_tile(half, C)

    def body(m_ref, r_ref, o_ref):
        o_ref[...] = ((m_ref[...].astype(F32) + r_ref[0].astype(F32)) + r_ref[1].astype(F32)) + r_ref[2].astype(F32)

    return _pcall(body, name=name, grid=(half // t,),
                  in_specs=[pl.BlockSpec((t, C), lambda i: (i, 0)), pl.BlockSpec((3, t, C), lambda i: (0, i, 0))],
                  out_specs=pl.BlockSpec((t, C), lambda i: (i, 0)), out_shape=jax.ShapeDtypeStruct((half, C), F32),
                  compiler_params=_params("parallel"))(mine, recv)


def _adamw(w, g, m, v, *, name):
    R, C = w.shape
    t = _row_tile(R, C)
    c1 = 1.0 - ADAM_B1 ** ADAM_STEP
    c2 = 1.0 - ADAM_B2 ** ADAM_STEP

    def body(w_ref, g_ref, m_ref, v_ref, d_ref, nm_ref, nv_ref):
        gv = g_ref[...]
        mn = ADAM_B1 * m_ref[...] + (1.0 - ADAM_B1) * gv
        vn = ADAM_B2 * v_ref[...] + (1.0 - ADAM_B2) * (gv * gv)
        d_ref[...] = -ADAM_LR * ((mn / c1) / (jnp.sqrt(vn / c2) + ADAM_EPS) + ADAM_WD * w_ref[...])
        nm_ref[...] = mn
        nv_ref[...] = vn

    blk = pl.BlockSpec((t, C), lambda i: (i, 0))
    shp = jax.ShapeDtypeStruct((R, C), F32)
    return _pcall(body, name=name, grid=(R // t,), in_specs=[blk] * 4, out_specs=[blk] * 3, out_shape=[shp] * 3,
                  compiler_params=_params("parallel"))(w, g, m, v)


BIG = ("w_in", "w_dil_out", "w_fox_out", "w_out", "w_ffn_in", "w_ffn_down")
SMALL = ("norm_mix_g", "b_fgt", "b_gate", "norm_ffn_g", "norm_final_g")
ORDER = ("norm_mix_g", "w_in", "b_fgt", "b_gate", "w_dil_out", "w_fox_out", "w_out", "norm_ffn_g", "w_ffn_in",
         "w_ffn_down", "norm_final_g")
SMALL_ROWS = {"norm_mix_g": (0, 1), "b_gate": (1, 3), "norm_ffn_g": (3, 4), "norm_final_g": (4, 5), "b_fgt": (5, 6)}


def _columns_to_blocks(full, ncol):
    K = full.shape[0]
    return full.reshape(K, 4, ncol).transpose(1, 0, 2)


def _blocks_to_columns(blocks):
    n, K, ncol = blocks.shape
    return blocks.transpose(1, 0, 2).reshape(K, n * ncol)


def kernel(x, norm_mix_g, w_in, b_fgt, b_gate, w_dil_out, w_fox_out, w_out, norm_ffn_g, w_ffn_in, w_ffn_down, norm_final_g, loss_target, m_norm_mix_g, m_w_in, m_b_fgt, m_b_gate, m_w_dil_out, m_w_fox_out, m_w_out, m_norm_ffn_g, m_w_ffn_in, m_w_ffn_down, m_norm_final_g, v_norm_mix_g, v_w_in, v_b_fgt, v_b_gate, v_w_dil_out, v_w_fox_out, v_w_out, v_norm_ffn_g, v_w_ffn_in, v_w_ffn_down, v_norm_final_g):
    weights = dict(norm_mix_g=norm_mix_g, w_in=w_in, b_fgt=b_fgt, b_gate=b_gate, w_dil_out=w_dil_out,
                   w_fox_out=w_fox_out, w_out=w_out, norm_ffn_g=norm_ffn_g, w_ffn_in=w_ffn_in, w_ffn_down=w_ffn_down,
                   norm_final_g=norm_final_g)
    m_in = dict(norm_mix_g=m_norm_mix_g, w_in=m_w_in, b_fgt=m_b_fgt, b_gate=m_b_gate, w_dil_out=m_w_dil_out,
                w_fox_out=m_w_fox_out, w_out=m_w_out, norm_ffn_g=m_norm_ffn_g, w_ffn_in=m_w_ffn_in,
                w_ffn_down=m_w_ffn_down, norm_final_g=m_norm_final_g)
    v_in = dict(norm_mix_g=v_norm_mix_g, w_in=v_w_in, b_fgt=v_b_fgt, b_gate=v_b_gate, w_dil_out=v_w_dil_out,
                w_fox_out=v_w_fox_out, w_out=v_w_out, norm_ffn_g=v_norm_ffn_g, w_ffn_in=v_w_ffn_in,
                w_ffn_down=v_w_ffn_down, norm_final_g=v_norm_final_g)
    c = lax.axis_index("c")
    chip = 2 * lax.axis_index("x") + lax.axis_index("y")

    shards = {n: weights[n][0].astype(_CD) for n in BIG}
    in_shape = jax.ShapeDtypeStruct((4,) + shards["w_in"].shape, _CD)
    send_i, recv_i, in_src, in_land, token_in = _split_copy_start(
        [shards["w_in"]], [in_shape], _gather_copies, norm_mix_g, name="gather_in_start")
    late = BIG[1:]
    send_g, recv_g, late_src, late_land, token = _split_copy_start(
        [shards[n] for n in late], [jax.ShapeDtypeStruct((4,) + shards[n].shape, _CD) for n in late],
        _gather_copies, token_in, name="gather_late_start")
    adam_in = [t[0] + token_in[0, 0] for t in (w_in, m_w_in, v_w_in)]
    p = dict(norm_mix_g=norm_mix_g, b_fgt=jnp.pad(b_fgt, ((0, 0), (0, F_PAD - N_FOX_HEADS))), b_gate=b_gate,
             norm_ffn_g=norm_ffn_g, norm_final_g=norm_final_g.reshape(1, D_MODEL))

    def first_weights(after):
        own, lands = _split_copy_wait(send_i, recv_i, in_src, in_land, _gather_copies, [after] + adam_in,
                                      name="gather_in_wait")
        (g_in,) = _forward_halves(lands, name="gather_in_forward")
        full_in = _blocks_to_columns(lax.dynamic_update_index_in_dim(g_in, own[0], chip, 0))
        o3 = QKV_COLS
        o4 = o3 + N_FOX_HEADS
        return dict(qkv=full_in[:, :o3], f=jnp.pad(full_in[:, o3:o4], ((0, 0), (0, F_PAD - N_FOX_HEADS))),
                    g=full_in[:, o4:])

    def late_weights(after):
        own, lands = _split_copy_wait(send_g, recv_g, late_src, late_land, _gather_copies, after,
                                      name="gather_late_wait")
        lands = _forward_halves(lands, name="gather_late_forward")
        g_dil, g_fox, g_out, g_ffn_in, g_ffn_down = [
            lax.dynamic_update_index_in_dim(l, s, chip, 0) for l, s in zip(lands, own)]
        return dict(dil_out=_blocks_to_columns(g_dil), fox_out=_blocks_to_columns(g_fox),
                    out=g_out.reshape(D_MODEL, D_MODEL), ffn_in=g_ffn_in,
                    ffn_down=g_ffn_down.reshape(D_FF, D_MODEL))

    c_arr = jnp.reshape(c, (1,)).astype(jnp.int32)

    def to_blocks(n, full):
        shape = weights[n].shape
        if full.ndim == 3:
            return full
        if n in ("w_out", "w_ffn_down"):
            return full.reshape(4, shape[1], shape[2])
        return _columns_to_blocks(full, shape[2])

    def pair_sums(group, named):
        names = list(named)
        blocks = [to_blocks(n, named[n]) for n in names]
        from_sibling = _swap_halves(blocks, name=f"swap_halves_{group}")
        return [_add_halves(b, r, c_arr, name=f"add_halves_{n}") for b, r, n in zip(blocks, from_sibling, names)]

    in_flight = {}

    def grad_sink(group, gw):
        if group == "in":
            named = {"w_in": jnp.concatenate([gw["qkv"], gw["f"][:, :N_FOX_HEADS], gw["g"]], axis=1)}
        else:
            named = {"w_" + k: v for k, v in gw.items()}
        sums = pair_sums(group, named)
        started = _split_copy_start(sums, [jax.ShapeDtypeStruct((3,) + s.shape[1:], s.dtype) for s in sums],
                                    _scatter_copies, next(iter(gw.values())), name=f"scatter_{group}_start")
        in_flight[group] = (list(named), started)
        return started[-1]

    loss_part, grad_x, gw, small = _layer_step(x[0], loss_target[0], {}, p, late_weights, grad_sink,
                                               (token_in, token), first_weights)

    def owner_sums(names, sums, from_chips):
        return {n: _add_owners(lax.dynamic_index_in_dim(s, chip, 0, keepdims=False), r, name=f"add_owners_{n}")
                for n, s, r in zip(names, sums, from_chips)}

    halves = {}
    for group, (names, (send_s, recv_s, srcs, lands, _)) in in_flight.items():
        sums, from_chips = _split_copy_wait(send_s, recv_s, srcs, lands, _scatter_copies, grad_x,
                                            name=f"scatter_{group}_wait")
        halves.update(owner_sums(names, sums, from_chips))
    halves = [halves[n] for n in BIG]
    grads = {}
    for n, own, other in zip(BIG, halves, _share_halves(halves)):
        pair = jnp.stack([own, other])
        grads[n] = jnp.where(c == 0, pair, pair[::-1]).reshape(2 * own.shape[0], own.shape[1])

    packed = jnp.concatenate([
        small["norm_mix_g"], small["b_gate"].reshape(2, D_MODEL), small["norm_ffn_g"], small["norm_final_g"],
        jnp.pad(small["b_fgt"], ((0, 0), (0, D_MODEL - F_PAD))), jnp.pad(loss_part, ((0, 0), (0, D_MODEL - 1))),
        jnp.zeros((1, D_MODEL), F32)], axis=0)
    summed = _sum_small(packed)
    for n in SMALL:
        lo, hi = SMALL_ROWS[n]
        grads[n] = summed[lo:hi].reshape(1, -1)[:, :weights[n].size]
    loss = summed[6, 0]

    out_g, out_d, out_m, out_v = {}, {}, {}, {}
    for n in ORDER:
        shape = weights[n].shape
        two_d = shape[1:] if len(shape) == 3 else (1, weights[n].size)
        g2 = grads[n].reshape(two_d)
        wmv = adam_in if n == "w_in" else [t.reshape(two_d) for t in (weights[n], m_in[n], v_in[n])]
        d2, m2, v2 = _adamw(wmv[0], g2, wmv[1], wmv[2], name=f"adamw_{n}")
        out_g[n], out_d[n], out_m[n], out_v[n] = (g2.reshape(shape), d2.reshape(shape), m2.reshape(shape),
                                                  v2.reshape(shape))
    return (loss, grad_x[None], *[out_g[n] for n in ORDER], *[out_d[n] for n in ORDER],
            *[out_m[n] for n in ORDER], *[out_v[n] for n in ORDER])
```

```python
import numpy as np
import jax
import jax.numpy as jnp
from jax import lax
from jax.experimental import pallas as pl
from jax.experimental.pallas import tpu as pltpu

F32 = jnp.float32
_CD = jnp.bfloat16

D_MODEL = 1024
HEAD_DIM = 64
DIL_PAIRS = ((128, 1), (512, 4), (2048, 16))
N_DIL_GROUPS = 3
DIL_HEADS = 4
DIL_W = 128
DIL_OUT = DIL_HEADS * HEAD_DIM
DIL_WIDTH = N_DIL_GROUPS * DIL_OUT
N_FOX_HEADS = 8
FOX_WIDTH = N_FOX_HEADS * HEAD_DIM
D_FF = 2816
QKV_COLS = 3 * DIL_WIDTH + 3 * FOX_WIDTH
F_PAD = 128
RMS_EPS = 1e-6
NEG_INF = -1e30
ATTN_SCALE = HEAD_DIM ** -0.5
ADAM_LR, ADAM_B1, ADAM_B2, ADAM_EPS, ADAM_WD, ADAM_STEP = 0.001, 0.9, 0.999, 1e-08, 0.01, 10

VMEM_LIMIT = 48 * 1024 * 1024
LANES = 128
MESH = pl.DeviceIdType.MESH
HBM_SPEC = pl.BlockSpec(memory_space=pltpu.HBM)


def _pcall(body, after=None, **kw):
    if after is None:
        return pl.pallas_call(body, **kw)
    n_in = len(kw["in_specs"])
    kw["in_specs"] = list(kw["in_specs"]) + [pl.BlockSpec(memory_space=pl.ANY)]

    def tied(*refs):
        return body(*refs[:n_in], *refs[n_in + 1:])

    call = pl.pallas_call(tied, **kw)
    return lambda *args: call(*args, after)


def _params(*sem):
    return pltpu.CompilerParams(dimension_semantics=sem, vmem_limit_bytes=VMEM_LIMIT)


def _pick(dim, pref):
    t = (min(pref, dim) // 128) * 128
    while t >= 128:
        if dim % t == 0:
            return t
        t -= 128
    return dim


def _mm(a, b, *, name, ta=False, tb=False, out_dtype=F32, add=None, tm=1024, tn=512, tk=2048, after=None,
        b_blocks=False, out_blocks=None, a_halves=False, b_halves=False):
    if a_halves:
        M, K = a.shape[1], 2 * a.shape[2]
    elif ta:
        K, M = a.shape
    else:
        M, K = a.shape
    if b_halves:
        b_rows, b_cols = b.shape[1], 2 * b.shape[2]
    else:
        b_rows, b_cols = (b.shape[1], b.shape[0] * b.shape[2]) if b_blocks else b.shape
    if tb:
        N, K2 = b_rows, b_cols
    else:
        K2, N = b_rows, b_cols
    assert K == K2, (a.shape, b.shape)
    shard = b.shape[2] if b_blocks else None
    tm = _pick(M, tm)
    tn = _pick(shard if (b_blocks and not tb) else (out_blocks or N), tn)
    tk = _pick(shard if (b_blocks and tb) else K, tk)
    nk = K // tk
    dn = (((0 if ta else 1,), (1 if tb else 0,)), ((), ()))
    has_add = add is not None
    assert not (has_add and out_blocks)

    def body(*refs):
        a_ref, b_ref = refs[0], refs[1]
        add_ref = refs[2] if has_add else None
        o_ref = refs[3] if has_add else refs[2]
        bv = b_ref[0] if b_blocks else b_ref[...]
        p = lax.dot_general(a_ref[...].astype(_CD), bv.astype(_CD), dn, preferred_element_type=F32)

        def finish(r):
            if has_add:
                r = r + add_ref[...]
            if out_blocks:
                o_ref[0] = r.astype(out_dtype)
            else:
                o_ref[...] = r.astype(out_dtype)

        if nk == 1:
            finish(p)
        else:
            acc_ref = refs[-1]
            k = pl.program_id(2)

            @pl.when(k == 0)
            def _():
                acc_ref[...] = p

            @pl.when(k > 0)
            def _():
                acc_ref[...] += p

            @pl.when(k == nk - 1)
            def _():
                finish(acc_ref[...])

    if a_halves:
        ka = (K // 2) // tk
        a_spec = pl.BlockSpec((None, tm, tk), lambda i, j, k: (k // ka, i, k % ka))
    else:
        a_spec = pl.BlockSpec((tk, tm), lambda i, j, k: (k, i)) if ta else pl.BlockSpec((tm, tk), lambda i, j, k: (i, k))
    if b_halves:
        nb_ = (N // 2) // tn
        b_spec = pl.BlockSpec((None, tk, tn), lambda i, j, k: (j // nb_, k, j % nb_))
    elif b_blocks and tb:
        per = shard // tk
        b_spec = pl.BlockSpec((1, tn, tk), lambda i, j, k: (k // per, j, k % per))
    elif b_blocks:
        per = shard // tn
        b_spec = pl.BlockSpec((1, tk, tn), lambda i, j, k: (j // per, k, j % per))
    else:
        b_spec = pl.BlockSpec((tn, tk), lambda i, j, k: (j, k)) if tb else pl.BlockSpec((tk, tn), lambda i, j, k: (k, j))
    if out_blocks:
        oper = out_blocks // tn
        o_spec = pl.BlockSpec((1, tm, tn), lambda i, j, k: (j // oper, i, j % oper))
        out_shape = jax.ShapeDtypeStruct((N // out_blocks, M, out_blocks), out_dtype)
    else:
        o_spec = pl.BlockSpec((tm, tn), lambda i, j, k: (i, j))
        out_shape = jax.ShapeDtypeStruct((M, N), out_dtype)
    in_specs = [a_spec, b_spec] + ([o_spec] if has_add else [])
    args = (a, b) + ((add,) if has_add else ())
    return _pcall(
        body, after, name=name, grid=(M // tm, N // tn, nk), in_specs=in_specs, out_specs=o_spec,
        out_shape=out_shape,
        scratch_shapes=[pltpu.VMEM((tm, tn), F32)] if nk > 1 else [],
        compiler_params=_params("parallel", "parallel", "arbitrary"),
    )(*args)


def _rms_fwd(x, g, *, name, tm=512, after=None):
    S, D = x.shape

    def body(x_ref, g_ref, h_ref):
        xv = x_ref[...]
        r = lax.rsqrt(jnp.mean(xv * xv, axis=-1, keepdims=True) + RMS_EPS)
        h_ref[...] = ((xv * r) * g_ref[...]).astype(h_ref.dtype)

    row = pl.BlockSpec((tm, D), lambda i: (i, 0))
    return _pcall(body, after, name=name, grid=(S // tm,), in_specs=[row, pl.BlockSpec((1, D), lambda i: (0, 0))],
                  out_specs=row, out_shape=jax.ShapeDtypeStruct((S, D), _CD), compiler_params=_params("parallel"))(x, g)


def _rms_bwd(x, g, dh, dres, *, name, tm=512, after=None):
    S, D = x.shape

    def body(x_ref, g_ref, dh_ref, dres_ref, dx_ref, dg_ref):
        xv = x_ref[...]
        r = lax.rsqrt(jnp.mean(xv * xv, axis=-1, keepdims=True) + RMS_EPS)
        xh = xv * r
        dhv = dh_ref[...]
        dxh = dhv * g_ref[...]
        dx_ref[...] = dres_ref[...] + r * (dxh - xh * jnp.mean(dxh * xh, axis=-1, keepdims=True))
        part = jnp.sum(dhv * xh, axis=0, keepdims=True)

        @pl.when(pl.program_id(0) == 0)
        def _():
            dg_ref[...] = part

        @pl.when(pl.program_id(0) > 0)
        def _():
            dg_ref[...] += part

    row = pl.BlockSpec((tm, D), lambda i: (i, 0))
    vec = pl.BlockSpec((1, D), lambda i: (0, 0))
    return _pcall(body, after, name=name, grid=(S // tm,), in_specs=[row, vec, row, row], out_specs=[row, vec],
                  out_shape=[jax.ShapeDtypeStruct((S, D), F32), jax.ShapeDtypeStruct((1, D), F32)],
                  compiler_params=_params("arbitrary"))(x, g, dh, dres)


def _loss_head(x, g, tgt, *, name, tm=512):
    S, D = x.shape

    def body(x_ref, g_ref, t_ref, loss_ref, dx_ref, dg_ref):
        xv = x_ref[...]
        gv = g_ref[...]
        r = lax.rsqrt(jnp.mean(xv * xv, axis=-1, keepdims=True) + RMS_EPS)
        xh = xv * r
        err = xh * gv - t_ref[...]
        lpart = 0.5 * jnp.sum(jnp.mean(err * err, axis=-1, keepdims=True), axis=0, keepdims=True)
        dy = err * (1.0 / D)
        dxh = dy * gv
        dx_ref[...] = r * (dxh - xh * jnp.mean(dxh * xh, axis=-1, keepdims=True))
        gpart = jnp.sum(dy * xh, axis=0, keepdims=True)

        @pl.when(pl.program_id(0) == 0)
        def _():
            loss_ref[...] = lpart
            dg_ref[...] = gpart

        @pl.when(pl.program_id(0) > 0)
        def _():
            loss_ref[...] += lpart
            dg_ref[...] += gpart

    row = pl.BlockSpec((tm, D), lambda i: (i, 0))
    vec = pl.BlockSpec((1, D), lambda i: (0, 0))
    one = pl.BlockSpec((1, 1), lambda i: (0, 0))
    return _pcall(body, name=name, grid=(S // tm,), in_specs=[row, vec, row], out_specs=[one, row, vec],
                  out_shape=[jax.ShapeDtypeStruct((1, 1), F32), jax.ShapeDtypeStruct((S, D), F32),
                             jax.ShapeDtypeStruct((1, D), F32)],
                  compiler_params=_params("arbitrary"))(x, g, tgt)


def _sigmoid(z):
    return 1.0 / (1.0 + jnp.exp(-z))


def _gate_fwd(gl, bg, ya, yb, *, name, tm=512):
    S, D = ya.shape

    def body(za_ref, zb_ref, ba_ref, bb_ref, ya_ref, yb_ref, o_ref):
        ga = _sigmoid(za_ref[...].astype(F32) + ba_ref[...])
        gb = _sigmoid(zb_ref[...].astype(F32) + bb_ref[...])
        o_ref[...] = (ga * ya_ref[...].astype(F32) + gb * yb_ref[...].astype(F32)).astype(o_ref.dtype)

    lo = pl.BlockSpec((tm, D), lambda i: (i, 0))
    hi = pl.BlockSpec((tm, D), lambda i: (i, 1))
    vlo = pl.BlockSpec((1, D), lambda i: (0, 0))
    vhi = pl.BlockSpec((1, D), lambda i: (0, 1))
    return _pcall(body, name=name, grid=(S // tm,), in_specs=[lo, hi, vlo, vhi, lo, lo], out_specs=lo,
                  out_shape=jax.ShapeDtypeStruct((S, D), _CD), compiler_params=_params("parallel"))(gl, gl, bg, bg, ya, yb)


def _gate_bwd(dm, gl, bg, ya, yb, *, name, tm=256):
    S, D = ya.shape

    def body(dm_ref, za_ref, zb_ref, ba_ref, bb_ref, ya_ref, yb_ref, dya_ref, dyb_ref, dgl_ref, dbg_ref):
        dmv = dm_ref[...].astype(F32)
        ga = _sigmoid(za_ref[...].astype(F32) + ba_ref[...])
        gb = _sigmoid(zb_ref[...].astype(F32) + bb_ref[...])
        dya_ref[...] = (dmv * ga).astype(dya_ref.dtype)
        dyb_ref[...] = (dmv * gb).astype(dyb_ref.dtype)
        dza = dmv * ya_ref[...].astype(F32) * ga * (1.0 - ga)
        dzb = dmv * yb_ref[...].astype(F32) * gb * (1.0 - gb)
        dgl_ref[:, :D] = dza.astype(dgl_ref.dtype)
        dgl_ref[:, D:] = dzb.astype(dgl_ref.dtype)
        pa = jnp.sum(dza, axis=0, keepdims=True)
        pb = jnp.sum(dzb, axis=0, keepdims=True)

        @pl.when(pl.program_id(0) == 0)
        def _():
            dbg_ref[:, :D] = pa
            dbg_ref[:, D:] = pb

        @pl.when(pl.program_id(0) > 0)
        def _():
            dbg_ref[:, :D] += pa
            dbg_ref[:, D:] += pb

    lo = pl.BlockSpec((tm, D), lambda i: (i, 0))
    hi = pl.BlockSpec((tm, D), lambda i: (i, 1))
    vlo = pl.BlockSpec((1, D), lambda i: (0, 0))
    vhi = pl.BlockSpec((1, D), lambda i: (0, 1))
    wide = pl.BlockSpec((tm, 2 * D), lambda i: (i, 0))
    vwide = pl.BlockSpec((1, 2 * D), lambda i: (0, 0))
    return _pcall(body, name=name, grid=(S // tm,), in_specs=[lo, lo, hi, vlo, vhi, lo, lo],
                  out_specs=[lo, lo, wide, vwide],
                  out_shape=[jax.ShapeDtypeStruct((S, D), _CD), jax.ShapeDtypeStruct((S, D), _CD),
                             jax.ShapeDtypeStruct((S, 2 * D), _CD), jax.ShapeDtypeStruct((1, 2 * D), F32)],
                  compiler_params=_params("arbitrary"))(dm, gl, gl, bg, bg, ya, yb)


def _ffn_in_act(h2, w_blocks, *, name, tm=512):
    S, D = h2.shape
    _, _, C = w_blocks.shape

    def body(a_ref, bg_ref, bu_ref, g_ref, u_ref, o_ref):
        av = a_ref[...].astype(_CD)
        gv = jnp.dot(av, bg_ref[0].astype(_CD), preferred_element_type=F32)
        uv = jnp.dot(av, bu_ref[0].astype(_CD), preferred_element_type=F32)
        g_ref[...] = gv.astype(g_ref.dtype)
        u_ref[...] = uv.astype(u_ref.dtype)
        o_ref[...] = (gv * _sigmoid(gv) * uv).astype(o_ref.dtype)

    out = pl.BlockSpec((tm, C), lambda i, j: (i, j))
    shp = jax.ShapeDtypeStruct((S, 2 * C), _CD)
    return _pcall(body, name=name, grid=(S // tm, 2),
                  in_specs=[pl.BlockSpec((tm, D), lambda i, j: (i, 0)), pl.BlockSpec((1, D, C), lambda i, j: (j, 0, 0)),
                            pl.BlockSpec((1, D, C), lambda i, j: (2 + j, 0, 0))],
                  out_specs=[out, out, out], out_shape=[shp, shp, shp],
                  compiler_params=_params("parallel", "arbitrary"))(h2, w_blocks, w_blocks)


def _d_swiglu(dx, w_down, gate, up, *, name, tm=512, tn=1408):
    S, D = dx.shape
    F = w_down.shape[0]
    nt = (((1,), (1,)), ((), ()))

    def body(a_ref, b_ref, g_ref, u_ref, o_ref):
        dv = lax.dot_general(a_ref[...].astype(_CD), b_ref[...].astype(_CD), nt, preferred_element_type=F32)
        gv = g_ref[...].astype(F32)
        sg = _sigmoid(gv)
        o_ref[0] = (dv * u_ref[...].astype(F32) * (sg * (1.0 + gv * (1.0 - sg)))).astype(o_ref.dtype)
        o_ref[1] = (dv * (gv * sg)).astype(o_ref.dtype)

    tile = pl.BlockSpec((tm, tn), lambda i, j: (i, j))
    return _pcall(body, name=name, grid=(S // tm, F // tn),
                  in_specs=[pl.BlockSpec((tm, D), lambda i, j: (i, 0)), pl.BlockSpec((tn, D), lambda i, j: (j, 0)),
                            tile, tile],
                  out_specs=pl.BlockSpec((2, tm, tn), lambda i, j: (0, i, j)),
                  out_shape=jax.ShapeDtypeStruct((2, S, F), _CD),
                  compiler_params=_params("parallel", "arbitrary"))(dx, w_down, gate, up)


def _split3(x):
    hi = x.astype(jnp.bfloat16)
    r1 = x - hi.astype(F32)
    mid = r1.astype(jnp.bfloat16)
    lo = (r1 - mid.astype(F32)).astype(jnp.bfloat16)
    return hi, mid, lo


def _ones_dot_left(ones, x):
    return sum(jnp.dot(ones, p, preferred_element_type=F32) for p in _split3(x))


def _ones_dot_right(x, ones):
    return sum(jnp.dot(p, ones, preferred_element_type=F32) for p in _split3(x))


def _head_sum(x):
    n = x.shape[1]
    r = lax.broadcasted_iota(jnp.int32, (n, n), 0) // HEAD_DIM
    c = lax.broadcasted_iota(jnp.int32, (n, n), 1) // HEAD_DIM
    return _ones_dot_right(x, (r == c).astype(jnp.bfloat16))


def _log_sigmoid(z):
    e = jnp.exp(-jnp.abs(z))
    t = 1.0 + e
    log1p_e = jnp.where(t == 1.0, e, jnp.log(t) * (e / jnp.where(t == 1.0, 1.0, t - 1.0)))
    return jnp.minimum(z, 0.0) - log1p_e


def _fox_cumsum(zf, bf, *, name):
    S, W = zf.shape
    nb = S // 128

    def body(z_ref, b_ref, c_ref):
        tri = (lax.broadcasted_iota(jnp.int32, (128, 128), 0) >= lax.broadcasted_iota(jnp.int32, (128, 128), 1))
        tri = tri.astype(jnp.bfloat16)

        def step(i, carry):
            rows = pl.ds(pl.multiple_of(i * 128, 128), 128)
            lf = _log_sigmoid(z_ref[rows, :] + b_ref[...])
            cb = _ones_dot_left(tri, lf) + carry
            c_ref[rows, :] = cb
            return cb[127:128, :]

        lax.fori_loop(0, nb, step, jnp.zeros((1, W), F32))

    return _pcall(body, name=name, out_shape=jax.ShapeDtypeStruct((S, W), F32),
                  compiler_params=pltpu.CompilerParams(vmem_limit_bytes=VMEM_LIMIT))(zf, bf)


def _fox_cumsum_bwd(dc, zf, bf, *, name):
    S, W = zf.shape
    nb = S // 128

    def body(dc_ref, z_ref, b_ref, dz_ref, db_ref):
        tri = (lax.broadcasted_iota(jnp.int32, (128, 128), 0) <= lax.broadcasted_iota(jnp.int32, (128, 128), 1))
        tri = tri.astype(jnp.bfloat16)

        def step(k, carry):
            tail, acc = carry
            i = nb - 1 - k
            rows = pl.ds(pl.multiple_of(i * 128, 128), 128)
            dlf = _ones_dot_left(tri, dc_ref[rows, :]) + tail
            dz = dlf * _sigmoid(-(z_ref[rows, :] + b_ref[...]))
            dz_ref[rows, :] = dz
            return dlf[0:1, :], acc + jnp.sum(dz, axis=0, keepdims=True)

        _, acc = lax.fori_loop(0, nb, step, (jnp.zeros((1, W), F32), jnp.zeros((1, W), F32)))
        db_ref[...] = acc

    return _pcall(body, name=name,
                  out_shape=[jax.ShapeDtypeStruct((S, W), F32), jax.ShapeDtypeStruct((1, W), F32)],
                  compiler_params=pltpu.CompilerParams(vmem_limit_bytes=VMEM_LIMIT))(dc, zf, bf)


def _proj_dil(h, w_qkv, *, name, tm=1024):
    S, D = h.shape
    tn = DIL_WIDTH

    def body(a_ref, b_ref, *rest):
        outs, acc = rest[:N_DIL_GROUPS], rest[N_DIL_GROUPS]
        prod = jnp.dot(a_ref[...].astype(_CD), b_ref[...].astype(_CD), preferred_element_type=F32)
        for k in range(tn // LANES):
            acc[k] = prod[:, k * LANES:(k + 1) * LANES]
        for g, (_, d) in enumerate(DIL_PAIRS):
            for half in range(DIL_OUT // LANES):
                k = g * (DIL_OUT // LANES) + half
                cols = slice(half * LANES, (half + 1) * LANES)
                for r in range(d):
                    rows = pl.ds(r, tm // d, stride=d) if d > 1 else slice(None)
                    outs[g][0, r, :, cols] = acc[k, rows, :].astype(outs[g].dtype)

    out_specs = [pl.BlockSpec((1, d, tm // d, DIL_OUT), lambda i, j: (j, 0, i, 0)) for _, d in DIL_PAIRS]
    out_shape = [jax.ShapeDtypeStruct((3, d, S // d, DIL_OUT), _CD) for _, d in DIL_PAIRS]
    outs = _pcall(body, name=name, grid=(S // tm, 3),
                  in_specs=[pl.BlockSpec((tm, D), lambda i, j: (i, 0)), pl.BlockSpec((D, tn), lambda i, j: (0, j))],
                  out_specs=out_specs, out_shape=out_shape, scratch_shapes=[pltpu.VMEM((tn // LANES, tm, LANES), F32)],
                  compiler_params=_params("parallel", "arbitrary"))(h, w_qkv)
    return [o.reshape(3, S, DIL_OUT) for o in outs]


def _dil_start(block, S, dilation):
    sub = S // dilation
    u0 = block * DIL_W
    return (u0 % sub) * dilation + u0 // sub


def _dil_slopes(group):
    h = np.arange(1, N_DIL_GROUPS * DIL_HEADS + 1, dtype=np.float32)
    s = (np.float32(2.0) ** (np.float32(-8.0) * h / np.float32(N_DIL_GROUPS * DIL_HEADS))).astype(np.float32)
    return [float(v) for v in s.reshape(N_DIL_GROUPS, DIL_HEADS)[group]]


def _dil_tiles(i, n, blocks_per_seq):
    qi = lax.broadcasted_iota(jnp.int32, (DIL_W, 2 * DIL_W), 0)
    kj = lax.broadcasted_iota(jnp.int32, (DIL_W, 2 * DIL_W), 1)
    rel = qi + DIL_W - kj
    first = ((4 * n + i) % blocks_per_seq) == 0
    valid = jnp.logical_and(jnp.logical_and(rel >= 0, rel <= DIL_W), jnp.logical_or(kj >= DIL_W, jnp.logical_not(first)))
    return valid, rel.astype(F32)


def _dil_window(cur_ref, prev_ref, i, cols):
    if i > 0:
        return cur_ref[(i - 1) * DIL_W:(i + 1) * DIL_W, cols]
    return jnp.concatenate([prev_ref[:, cols], cur_ref[:DIL_W, cols]], axis=0)


CHUNK = 4 * DIL_W


def _dil_rows(block, S, dilation):
    start = _dil_start(block, S, dilation)
    return pl.ds(start, DIL_W, stride=dilation) if dilation > 1 else pl.ds(start, DIL_W)


def SPLIT(S):
    return (DIL_OUT // LANES, S, LANES)


def _dil_fwd(qkv, group, *, name):
    S = qkv.shape[1]
    dilation = DIL_PAIRS[group][1]
    bps = (S // dilation) // DIL_W
    slopes = _dil_slopes(group)
    nt = (((1,), (1,)), ((), ()))

    def body(q_ref, k_ref, v_ref, kp_ref, vp_ref, on_ref, ln_ref, o_ref, l_ref):
        n = pl.program_id(0)
        for i in range(4):
            valid, rel = _dil_tiles(i, n, bps)
            rows = slice(i * DIL_W, (i + 1) * DIL_W)
            for h in range(DIL_HEADS):
                cols = slice(h * HEAD_DIM, (h + 1) * HEAD_DIM)
                qh = q_ref[rows, cols]
                k2, v2 = _dil_window(k_ref, kp_ref, i, cols), _dil_window(v_ref, vp_ref, i, cols)
                s = lax.dot_general(qh, k2, nt, preferred_element_type=F32) * ATTN_SCALE - (slopes[h] * dilation) * rel
                s = jnp.where(valid, s, NEG_INF)
                m = jnp.max(s, axis=-1, keepdims=True)
                p = jnp.exp(s - m)
                den = jnp.sum(p, axis=-1, keepdims=True)
                acc = jnp.dot(p.astype(_CD), v2, preferred_element_type=F32)
                o_ref[rows, cols] = acc / den
                l_ref[rows, cols] = jnp.broadcast_to(m + jnp.log(den), (DIL_W, HEAD_DIM))
        for i in range(4):
            rows = slice(i * DIL_W, (i + 1) * DIL_W)
            nat = _dil_rows(4 * n + i, S, dilation)
            for half in range(DIL_OUT // LANES):
                cols = slice(half * LANES, (half + 1) * LANES)
                on_ref[half, nat, :] = o_ref[rows, cols]
                ln_ref[half, nat, :] = l_ref[rows, cols]

    def cur(which):
        return pl.BlockSpec((None, CHUNK, DIL_OUT), lambda n: (which, n, 0))

    def prev(which):
        return pl.BlockSpec((None, DIL_W, DIL_OUT), lambda n: (which, jnp.maximum(4 * n - 1, 0), 0))

    whole = pl.BlockSpec(SPLIT(S), lambda n: (0, 0, 0))
    return _pcall(body, name=name, grid=(S // CHUNK,), in_specs=[cur(0), cur(1), cur(2), prev(1), prev(2)],
                  out_specs=[whole, whole],
                  out_shape=[jax.ShapeDtypeStruct(SPLIT(S), F32), jax.ShapeDtypeStruct(SPLIT(S), F32)],
                  scratch_shapes=[pltpu.VMEM((CHUNK, DIL_OUT), F32), pltpu.VMEM((CHUNK, DIL_OUT), F32)],
                  compiler_params=_params("arbitrary"))(qkv, qkv, qkv, qkv, qkv)


def _dil_bwd(qkv, o, lse, do, dlse, group, *, name):
    S = qkv.shape[1]
    dilation = DIL_PAIRS[group][1]
    bps = (S // dilation) // DIL_W
    slopes = _dil_slopes(group)
    nchunk = S // CHUNK
    nt = (((1,), (1,)), ((), ()))
    tn = (((0,), (0,)), ((), ()))

    def body(q_ref, k_ref, v_ref, kp_ref, vp_ref, on_ref, ln_ref, don_ref, dln_ref, dq_ref, dk_ref, dv_ref,
             dk_s, dv_s, o_ref, l_ref, do_ref, dl_ref):
        step = pl.program_id(0)
        n = nchunk - 1 - step
        for i in range(4):
            rows = slice(i * DIL_W, (i + 1) * DIL_W)
            nat = _dil_rows(4 * n + i, S, dilation)
            for half in range(DIL_OUT // LANES):
                cols = slice(half * LANES, (half + 1) * LANES)
                o_ref[rows, cols] = on_ref[half, nat, :]
                l_ref[rows, cols] = ln_ref[half, nat, :]
                do_ref[rows, cols] = don_ref[half, nat, :]
                dl_ref[rows, cols] = dln_ref[half, nat, :]

        @pl.when(step == 0)
        def _():
            dk_s[CHUNK:, :] = jnp.zeros((DIL_W, DIL_OUT), F32)
            dv_s[CHUNK:, :] = jnp.zeros((DIL_W, DIL_OUT), F32)

        dk_s[:CHUNK, :] = jnp.zeros((CHUNK, DIL_OUT), F32)
        dv_s[:CHUNK, :] = jnp.zeros((CHUNK, DIL_OUT), F32)
        for i in range(4):
            valid, rel = _dil_tiles(i, n, bps)
            rows = slice(i * DIL_W, (i + 1) * DIL_W)
            window = slice(i * DIL_W, (i + 2) * DIL_W)
            for h in range(DIL_HEADS):
                cols = slice(h * HEAD_DIM, (h + 1) * HEAD_DIM)
                qh = q_ref[rows, cols]
                k2, v2 = _dil_window(k_ref, kp_ref, i, cols), _dil_window(v_ref, vp_ref, i, cols)
                lh = l_ref[rows, h * HEAD_DIM:h * HEAD_DIM + 1]
                s = lax.dot_general(qh, k2, nt, preferred_element_type=F32) * ATTN_SCALE - (slopes[h] * dilation) * rel
                p = jnp.exp(jnp.where(valid, s, NEG_INF) - lh)
                doh = do_ref[rows, cols]
                dsum = jnp.sum(doh * o_ref[rows, cols], axis=-1, keepdims=True)
                shift = dl_ref[rows, h * HEAD_DIM:h * HEAD_DIM + 1] - dsum
                dob = doh.astype(_CD)
                ds = p * (lax.dot_general(dob, v2, nt, preferred_element_type=F32) + shift)
                dsb = (ds * ATTN_SCALE).astype(_CD)
                dq_ref[rows, cols] = jnp.dot(dsb, k2, preferred_element_type=F32).astype(dq_ref.dtype)
                dk_s[window, cols] += lax.dot_general(dsb, qh, tn, preferred_element_type=F32)
                dv_s[window, cols] += lax.dot_general(p.astype(_CD), dob, tn, preferred_element_type=F32)
        dk_ref[...] = dk_s[DIL_W:, :].astype(dk_ref.dtype)
        dv_ref[...] = dv_s[DIL_W:, :].astype(dv_ref.dtype)
        dk_s[CHUNK:, :] = dk_s[:DIL_W, :]
        dv_s[CHUNK:, :] = dv_s[:DIL_W, :]

    def cur(which):
        return pl.BlockSpec((None, CHUNK, DIL_OUT), lambda s: (which, nchunk - 1 - s, 0))

    def prev(which):
        return pl.BlockSpec((None, DIL_W, DIL_OUT), lambda s: (which, jnp.maximum(4 * (nchunk - 1 - s) - 1, 0), 0))

    whole = pl.BlockSpec(SPLIT(S), lambda s: (0, 0, 0))
    out = pl.BlockSpec((CHUNK, DIL_OUT), lambda s: (nchunk - 1 - s, 0))
    shp = jax.ShapeDtypeStruct((S, DIL_OUT), _CD)
    tile = pltpu.VMEM((CHUNK, DIL_OUT), F32)
    return _pcall(body, name=name, grid=(nchunk,),
                  in_specs=[cur(0), cur(1), cur(2), prev(1), prev(2), whole, whole, whole, whole],
                  out_specs=[out, out, out], out_shape=[shp, shp, shp],
                  scratch_shapes=[pltpu.VMEM((CHUNK + DIL_W, DIL_OUT), F32), pltpu.VMEM((CHUNK + DIL_W, DIL_OUT), F32),
                                  tile, tile, tile, tile],
                  compiler_params=_params("arbitrary"))(qkv, qkv, qkv, qkv, qkv, o, lse, do, dlse)


def _dil_mix_fwd(os_, ls_, *, name, tm=512):
    nh, S, _ = os_[0].shape

    def body(o0, o1, o2, l0, l1, l2, out_ref):
        for half in range(nh):
            ls = [l0[half], l1[half], l2[half]]
            m = jnp.maximum(jnp.maximum(ls[0], ls[1]), ls[2])
            es = [jnp.exp(l - m) for l in ls]
            den = es[0] + es[1] + es[2]
            mixed = (es[0] * o0[half] + es[1] * o1[half] + es[2] * o2[half]) / den
            out_ref[:, half * LANES:(half + 1) * LANES] = mixed.astype(out_ref.dtype)

    halves = pl.BlockSpec((nh, tm, LANES), lambda i: (0, i, 0))
    row = pl.BlockSpec((tm, nh * LANES), lambda i: (i, 0))
    return _pcall(body, name=name, grid=(S // tm,), in_specs=[halves] * 6, out_specs=row,
                  out_shape=jax.ShapeDtypeStruct((S, nh * LANES), _CD), compiler_params=_params("parallel"))(*os_, *ls_)


def _dil_mix_bwd(doa, os_, ls_, *, name, tm=512, after=None):
    nh, S, _ = os_[0].shape

    def body(d_ref, o0, o1, o2, l0, l1, l2, do0, do1, do2, dl0, dl1, dl2):
        for half in range(nh):
            dv = d_ref[:, half * LANES:(half + 1) * LANES]
            ls = [l0[half], l1[half], l2[half]]
            m = jnp.maximum(jnp.maximum(ls[0], ls[1]), ls[2])
            es = [jnp.exp(l - m) for l in ls]
            den = es[0] + es[1] + es[2]
            al = [e / den for e in es]
            da = [_head_sum(dv * o[half]) for o in (o0, o1, o2)]
            mean = al[0] * da[0] + al[1] * da[1] + al[2] * da[2]
            for a, d_, do_ref, dl_ref in zip(al, da, (do0, do1, do2), (dl0, dl1, dl2)):
                do_ref[half] = a * dv
                dl_ref[half] = a * (d_ - mean)

    halves = pl.BlockSpec((nh, tm, LANES), lambda i: (0, i, 0))
    row = pl.BlockSpec((tm, nh * LANES), lambda i: (i, 0))
    shp = jax.ShapeDtypeStruct((nh, S, LANES), F32)
    return _pcall(body, after, name=name, grid=(S // tm,), in_specs=[row] + [halves] * 6, out_specs=[halves] * 6,
                  out_shape=[shp] * 6, compiler_params=_params("parallel"))(doa, *os_, *ls_)


FOX_T = 512


PACK = 2 * HEAD_DIM
HEAD_PAIRS = N_FOX_HEADS // 2
FOX_HPS = 8
Q_BLOCK0 = 0
K_BLOCK0 = FOX_WIDTH // PACK
V_BLOCK0 = 2 * FOX_WIDTH // PACK


def _pieces(x):
    hi = x.astype(jnp.bfloat16).astype(F32)
    r = x - hi
    mid = r.astype(jnp.bfloat16).astype(F32)
    lo = (r - mid).astype(jnp.bfloat16).astype(F32)
    return [hi, mid, lo]


def _extras(first, second, rows):
    lane = lax.broadcasted_iota(jnp.int32, (rows, HEAD_DIM), 1)
    out = jnp.zeros((rows, HEAD_DIM), F32)
    for idx, val in enumerate(list(first) + list(second)):
        out = jnp.where(lane == idx, val, out)
    return out


def _head_column(c, h):
    lane = lax.broadcasted_iota(jnp.int32, c.shape, 1)
    return jnp.sum(jnp.where(lane == h, c, 0.0), axis=1, keepdims=True)


ONES3 = [1.0, 1.0, 1.0]
ZEROS3 = [0.0, 0.0, 0.0]


def _fox_pack_fwd(qkv, c, *, name, tm=512):
    S = qkv.shape[0]

    def body(q_ref, k_ref, v_ref, c_ref, qo_ref, ko_ref, vo_ref):
        hp = pl.program_id(1)
        cv = c_ref[...]
        for hh in range(2):
            ch = _pieces(_head_column(cv, 2 * hp + hh))
            src = slice(hh * HEAD_DIM, (hh + 1) * HEAD_DIM)
            lo = slice(hh * PACK, hh * PACK + HEAD_DIM)
            hi = slice(hh * PACK + HEAD_DIM, (hh + 1) * PACK)
            qo_ref[:, lo] = (q_ref[:, src].astype(F32) * ATTN_SCALE).astype(qo_ref.dtype)
            qo_ref[:, hi] = _extras(ch, ONES3, tm).astype(qo_ref.dtype)
            ko_ref[:, lo] = k_ref[:, src]
            ko_ref[:, hi] = _extras(ONES3, [-p for p in ch], tm).astype(ko_ref.dtype)
            vo_ref[:, lo] = v_ref[:, src]
            vo_ref[:, hi] = _extras(ONES3, ZEROS3, tm).astype(vo_ref.dtype)

    def src(block0):
        return pl.BlockSpec((tm, PACK), lambda i, hp: (i, block0 + hp))

    out = pl.BlockSpec((tm, 2 * PACK), lambda i, hp: (i, hp))
    shp = jax.ShapeDtypeStruct((S, N_FOX_HEADS * PACK), _CD)
    return _pcall(body, name=name, grid=(S // tm, HEAD_PAIRS),
                  in_specs=[src(Q_BLOCK0), src(K_BLOCK0), src(V_BLOCK0), pl.BlockSpec((tm, PACK), lambda i, hp: (i, 0))],
                  out_specs=[out, out, out], out_shape=[shp, shp, shp],
                  compiler_params=_params("parallel", "parallel"))(qkv, qkv, qkv, c)


def _fox_fwd(qp, kp, vp, *, name):
    S = qp.shape[0]
    nt = S // FOX_T
    nt_dims = (((1,), (1,)), ((), ()))
    tn_dims = (((0,), (0,)), ((), ()))

    def body(i_tab, j_tab, q_ref, k_ref, v_ref, o_ref, l_ref, m_s, acc_s):
        t = pl.program_id(1)
        i, j = i_tab[t], j_tab[t]

        @pl.when(j == 0)
        def _():
            m_s[...] = jnp.full((FOX_HPS, 1, FOX_T), NEG_INF, F32)
            acc_s[...] = jnp.zeros((FOX_HPS, PACK, FOX_T), F32)

        def tile(diagonal):
            for hh in range(FOX_HPS):
                cols = slice(hh * PACK, (hh + 1) * PACK)
                st = lax.dot_general(k_ref[:, cols], q_ref[:, cols], nt_dims, preferred_element_type=F32)
                if diagonal:
                    key = lax.broadcasted_iota(jnp.int32, (FOX_T, FOX_T), 0)
                    qry = lax.broadcasted_iota(jnp.int32, (FOX_T, FOX_T), 1)
                    st = jnp.where(key <= qry, st, NEG_INF)
                m_old = m_s[hh]
                m_new = jnp.maximum(m_old, jnp.max(st, axis=0, keepdims=True))
                pt = jnp.exp(st - m_new)
                acc_s[hh] = jnp.exp(m_old - m_new) * acc_s[hh] + lax.dot_general(
                    v_ref[:, cols], pt.astype(_CD), tn_dims, preferred_element_type=F32)
                m_s[hh] = m_new

        @pl.when(j < i)
        def _():
            tile(False)

        @pl.when(j == i)
        def _():
            tile(True)
            for hh in range(FOX_HPS):
                acc = acc_s[hh]
                den = acc[HEAD_DIM:HEAD_DIM + 1, :]
                cols = slice(hh * HEAD_DIM, (hh + 1) * HEAD_DIM)
                o_ref[:, cols] = (acc[:HEAD_DIM, :] / den).T
                l_ref[:, cols] = jnp.broadcast_to(m_s[hh] + jnp.log(den), (HEAD_DIM, FOX_T)).T

    pairs = [(i, j) for i in range(nt) for j in range(i + 1)]
    i_tab = jnp.asarray([p[0] for p in pairs], jnp.int32)
    j_tab = jnp.asarray([p[1] for p in pairs], jnp.int32)
    qs = pl.BlockSpec((FOX_T, FOX_HPS * PACK), lambda hp, t, it, jt: (it[t], hp))
    ks = pl.BlockSpec((FOX_T, FOX_HPS * PACK), lambda hp, t, it, jt: (jt[t], hp))
    os_ = pl.BlockSpec((FOX_T, FOX_HPS * HEAD_DIM), lambda hp, t, it, jt: (it[t], hp))
    shp = jax.ShapeDtypeStruct((S, FOX_WIDTH), F32)
    grid_spec = pltpu.PrefetchScalarGridSpec(
        num_scalar_prefetch=2, grid=(N_FOX_HEADS // FOX_HPS, len(pairs)), in_specs=[qs, ks, ks], out_specs=[os_, os_],
        scratch_shapes=[pltpu.VMEM((FOX_HPS, 1, FOX_T), F32), pltpu.VMEM((FOX_HPS, PACK, FOX_T), F32)])
    return _pcall(body, name=name, grid_spec=grid_spec, out_shape=[shp, shp],
                  compiler_params=_params("parallel", "arbitrary"))(i_tab, j_tab, qp, kp, vp)


def _fox_pack_bwd(qkv, c, o, lse, do, *, name, tm=512, after=None):
    S = qkv.shape[0]

    def body(q_ref, c_ref, o_ref, l_ref, do_ref, qo_ref, do_out_ref):
        hp = pl.program_id(1)
        cv = c_ref[...]
        for hh in range(2):
            src = slice(hh * HEAD_DIM, (hh + 1) * HEAD_DIM)
            lo = slice(hh * PACK, hh * PACK + HEAD_DIM)
            hi = slice(hh * PACK + HEAD_DIM, (hh + 1) * PACK)
            shift = _head_column(cv, 2 * hp + hh) - l_ref[:, hh * HEAD_DIM:hh * HEAD_DIM + 1]
            dov = do_ref[:, src]
            dsum = jnp.sum(dov * o_ref[:, src], axis=-1, keepdims=True)
            qo_ref[:, lo] = (q_ref[:, src].astype(F32) * ATTN_SCALE).astype(qo_ref.dtype)
            qo_ref[:, hi] = _extras(_pieces(shift), ONES3, tm).astype(qo_ref.dtype)
            do_out_ref[:, lo] = dov.astype(do_out_ref.dtype)
            do_out_ref[:, hi] = _extras(_pieces(-dsum), ZEROS3, tm).astype(do_out_ref.dtype)

    pair = pl.BlockSpec((tm, PACK), lambda i, hp: (i, hp))
    out = pl.BlockSpec((tm, 2 * PACK), lambda i, hp: (i, hp))
    shp = jax.ShapeDtypeStruct((S, N_FOX_HEADS * PACK), _CD)
    return _pcall(body, after, name=name, grid=(S // tm, HEAD_PAIRS),
                  in_specs=[pl.BlockSpec((tm, PACK), lambda i, hp: (i, Q_BLOCK0 + hp)),
                            pl.BlockSpec((tm, PACK), lambda i, hp: (i, 0)), pair, pair, pair],
                  out_specs=[out, out], out_shape=[shp, shp],
                  compiler_params=_params("parallel", "parallel"))(qkv, c, o, lse, do)


def _fox_bwd(qp, kp, vp, dop, *, name):
    S = qp.shape[0]
    nt = S // FOX_T
    nt_dims = (((1,), (1,)), ((), ()))
    tn_dims = (((0,), (0,)), ((), ()))

    def body(i_tab, j_tab, q_ref, k_ref, v_ref, do_ref, dq_ref, dk_ref, dv_ref, dc_ref, dr_ref,
             dq_s, dk_s, dv_s, dc_s, dr_s):
        t = pl.program_id(1)
        i, j = i_tab[t], j_tab[t]

        @pl.when(t == 0)
        def _():
            dq_s[...] = jnp.zeros((S, FOX_HPS * PACK), F32)
            dr_s[...] = jnp.zeros((FOX_HPS, 1, S), F32)

        @pl.when(i == j)
        def _():
            dk_s[...] = jnp.zeros((FOX_T, FOX_HPS * PACK), F32)
            dv_s[...] = jnp.zeros((FOX_T, FOX_HPS * PACK), F32)
            dc_s[...] = jnp.zeros((FOX_HPS, FOX_T, 1), F32)

        def tile(diagonal):
            rows = pl.ds(pl.multiple_of(i * FOX_T, FOX_T), FOX_T)
            for hh in range(FOX_HPS):
                cols = slice(hh * PACK, (hh + 1) * PACK)
                qv, kv, vv, dov = q_ref[:, cols], k_ref[:, cols], v_ref[:, cols], do_ref[:, cols]
                pt = jnp.exp(lax.dot_general(kv, qv, nt_dims, preferred_element_type=F32))
                if diagonal:
                    key = lax.broadcasted_iota(jnp.int32, (FOX_T, FOX_T), 0)
                    qry = lax.broadcasted_iota(jnp.int32, (FOX_T, FOX_T), 1)
                    pt = jnp.where(key <= qry, pt, 0.0)
                dst = pt * lax.dot_general(vv, dov, nt_dims, preferred_element_type=F32)
                dsb = dst.astype(_CD)
                dc_s[hh] += jnp.sum(dst, axis=1, keepdims=True)
                dr_s[hh, :, rows] += jnp.sum(dst, axis=0, keepdims=True)
                dv_s[:, cols] += jnp.dot(pt.astype(_CD), dov, preferred_element_type=F32)
                dk_s[:, cols] += jnp.dot(dsb, qv, preferred_element_type=F32)
                dq_s[rows, cols] += lax.dot_general(dsb, kv, tn_dims, preferred_element_type=F32)

        @pl.when(i > j)
        def _():
            tile(False)

        @pl.when(i == j)
        def _():
            tile(True)

        @pl.when(i == nt - 1)
        def _():
            for hh in range(FOX_HPS):
                src = slice(hh * PACK, hh * PACK + HEAD_DIM)
                dst_cols = slice(hh * HEAD_DIM, (hh + 1) * HEAD_DIM)
                dk_ref[:, dst_cols] = dk_s[:, src].astype(dk_ref.dtype)
                dv_ref[:, dst_cols] = dv_s[:, src].astype(dv_ref.dtype)
                dc_ref[:, dst_cols] = jnp.broadcast_to(dc_s[hh], (FOX_T, HEAD_DIM))

        @pl.when(t == len(pairs) - 1)
        def _():
            for hh in range(FOX_HPS):
                dq_ref[:, hh * HEAD_DIM:(hh + 1) * HEAD_DIM] = (
                    dq_s[:, hh * PACK:hh * PACK + HEAD_DIM] * ATTN_SCALE).astype(dq_ref.dtype)
            dr_ref[...] = dr_s[...]

    pairs = [(i, j) for j in range(nt) for i in range(j, nt)]
    i_tab = jnp.asarray([p[0] for p in pairs], jnp.int32)
    j_tab = jnp.asarray([p[1] for p in pairs], jnp.int32)
    wide, narrow = FOX_HPS * PACK, FOX_HPS * HEAD_DIM
    qs = pl.BlockSpec((FOX_T, wide), lambda hp, t, it, jt: (it[t], hp))
    ks = pl.BlockSpec((FOX_T, wide), lambda hp, t, it, jt: (jt[t], hp))
    whole = pl.BlockSpec((S, narrow), lambda hp, t, it, jt: (0, hp))
    cs = pl.BlockSpec((FOX_T, narrow), lambda hp, t, it, jt: (jt[t], hp))
    rs = pl.BlockSpec((FOX_HPS, 1, S), lambda hp, t, it, jt: (hp, 0, 0))
    shp = jax.ShapeDtypeStruct((S, FOX_WIDTH), _CD)
    grid_spec = pltpu.PrefetchScalarGridSpec(
        num_scalar_prefetch=2, grid=(N_FOX_HEADS // FOX_HPS, len(pairs)), in_specs=[qs, ks, ks, qs],
        out_specs=[whole, cs, cs, cs, rs],
        scratch_shapes=[pltpu.VMEM((S, wide), F32), pltpu.VMEM((FOX_T, wide), F32),
                        pltpu.VMEM((FOX_T, wide), F32), pltpu.VMEM((FOX_HPS, FOX_T, 1), F32),
                        pltpu.VMEM((FOX_HPS, 1, S), F32)])
    return _pcall(body, name=name, grid_spec=grid_spec,
                  out_shape=[shp, shp, shp, jax.ShapeDtypeStruct((S, FOX_WIDTH), F32),
                             jax.ShapeDtypeStruct((N_FOX_HEADS, 1, S), F32)],
                  compiler_params=_params("parallel", "arbitrary"))(i_tab, j_tab, qp, kp, vp, dop)


def _redilate(t, d):
    if d == 1:
        return t
    S, C = t.shape
    return t.reshape(d, S // d, C).transpose(1, 0, 2).reshape(S, C)


def _layer_step(x, tgt, w, p, late_weights=None, grad_sink=None, after=None, first_weights=None):
    S = x.shape[0]
    after_norm, after_proj = after if after is not None else (None, None)
    h = _rms_fwd(x, p["norm_mix_g"], name="rms_mix", after=after_norm)
    if first_weights is not None:
        w = {**w, **first_weights(h)}
    qkv = _mm(h, w["qkv"][:, 3 * DIL_WIDTH:], name="proj_fox", out_dtype=_CD, tn=768, tm=2048, after=after_proj)
    dil_qkv = _proj_dil(h, w["qkv"], name="proj_dil")
    zf = _mm(h, w["f"], name="proj_f")
    gl = _mm(h, w["g"], name="proj_gate", tn=1024, out_dtype=_CD)

    dil_o, dil_l = [], []
    for g in range(N_DIL_GROUPS):
        og, lg = _dil_fwd(dil_qkv[g], g, name=f"dil_fwd{g}")
        dil_o.append(og), dil_l.append(lg)
    o_a = _dil_mix_fwd(dil_o, dil_l, name="dil_mix")

    c = _fox_cumsum(zf, p["b_fgt"], name="fox_cumsum")
    fqp, fkp, fvp = _fox_pack_fwd(qkv, c, name="fox_pack")
    o_b, flse = _fox_fwd(fqp, fkp, fvp, name="fox_fwd")

    if late_weights is not None:
        w = {**w, **late_weights(o_b)}
    y_a = _mm(o_a, w["dil_out"], name="y_a", tn=1024, out_dtype=_CD)
    y_b = _mm(o_b, w["fox_out"], name="y_b", tn=1024, out_dtype=_CD)
    merged = _gate_fwd(gl, p["b_gate"], y_a, y_b, name="gate_fwd")
    x1 = _mm(merged, w["out"], name="mix_out", add=x)

    h2 = _rms_fwd(x1, p["norm_ffn_g"], name="rms_ffn")
    gate, up, act = _ffn_in_act(h2, w["ffn_in"], name="ffn_in")
    x2 = _mm(act, w["ffn_down"], name="ffn_down", add=x1, tk=2816)

    loss, dx2, dg_final = _loss_head(x2, p["norm_final_g"], tgt, name="loss_head")

    gw_ffn_down = _mm(act, dx2, name="gw_ffn_down", ta=True, out_dtype=_CD, tm=1408)
    dgu = _d_swiglu(dx2, w["ffn_down"], gate, up, name="d_swiglu")
    dh2 = _mm(dgu, w["ffn_in"], name="d_h2", tb=True, tk=1408, b_blocks=True, tm=2048, a_halves=True)
    gw_ffn_in = _mm(h2, dgu, name="gw_ffn_in", ta=True, out_dtype=_CD, tn=1408, out_blocks=1408, b_halves=True)
    sink = grad_sink if grad_sink is not None else (lambda group, grads: None)
    tok = sink("ffn", dict(ffn_in=gw_ffn_in, ffn_down=gw_ffn_down))
    dx1, dg_ffn = _rms_bwd(x1, p["norm_ffn_g"], dh2, dx2, name="rms_ffn_bwd", after=tok)

    dmerged = _mm(dx1, w["out"], name="d_merged", tb=True, out_dtype=_CD)
    gw_out = _mm(merged, dx1, name="gw_out", ta=True, out_dtype=_CD)
    dy_a, dy_b, dgl, db_gate = _gate_bwd(dmerged, gl, p["b_gate"], y_a, y_b, name="gate_bwd")
    do_a = _mm(dy_a, w["dil_out"], name="d_o_a", tb=True)
    gw_dil_out = _mm(o_a, dy_a, name="gw_dil_out", ta=True, out_dtype=_CD, tn=1024)
    do_b = _mm(dy_b, w["fox_out"], name="d_o_b", tb=True)
    gw_fox_out = _mm(o_b, dy_b, name="gw_fox_out", ta=True, out_dtype=_CD, tn=1024)
    tok = sink("mix", dict(dil_out=gw_dil_out, fox_out=gw_fox_out, out=gw_out))

    bqp, bdop = _fox_pack_bwd(qkv, c, o_b, flse, do_b, name="fox_pack_bwd", after=tok)
    dqp, dkp, dvp, dck, dcq = _fox_bwd(bqp, fkp, fvp, bdop, name="fox_bwd")
    dc = dcq[:, 0, :].T - dck.reshape(S, N_FOX_HEADS, HEAD_DIM)[:, :, 0]
    dc = jnp.pad(dc, ((0, 0), (0, F_PAD - N_FOX_HEADS)))
    dzf, db_fgt = _fox_cumsum_bwd(dc, zf, p["b_fgt"], name="fox_cumsum_bwd")

    douts = _dil_mix_bwd(do_a, dil_o, dil_l, name="dil_mix_bwd", after=tok)
    dqs, dks, dvs = [], [], []
    for g, (_, d) in enumerate(DIL_PAIRS):
        dq, dk, dv = _dil_bwd(dil_qkv[g], dil_o[g], dil_l[g], douts[g], douts[3 + g], g, name=f"dil_bwd{g}")
        dqs.append(_redilate(dq, d)), dks.append(_redilate(dk, d)), dvs.append(_redilate(dv, d))
    dqkv = jnp.concatenate(dqs + dks + dvs + [dqp, dkp, dvp], axis=1)

    gw_qkv = _mm(h, dqkv, name="gw_qkv", ta=True, out_dtype=_CD, tn=768)
    gw_g = _mm(h, dgl, name="gw_gate", ta=True, out_dtype=_CD)
    gw_f = _mm(h, dzf, name="gw_f", ta=True, out_dtype=_CD)
    tok = sink("in", dict(qkv=gw_qkv, f=gw_f, g=gw_g))
    dh = _mm(dqkv, w["qkv"], name="d_h_qkv", tb=True, tk=1920, tm=2048, after=tok)
    dh = _mm(dgl, w["g"], name="d_h_gate", tb=True, add=dh)
    dh = _mm(dzf, w["f"], name="d_h_f", tb=True, add=dh)
    dx, dg_mix = _rms_bwd(x, p["norm_mix_g"], dh, dx1, name="rms_mix_bwd")

    gw = dict(qkv=gw_qkv, f=gw_f, g=gw_g, dil_out=gw_dil_out, fox_out=gw_fox_out, out=gw_out, ffn_in=gw_ffn_in,
              ffn_down=gw_ffn_down)
    small = dict(norm_mix_g=dg_mix, b_fgt=db_fgt, b_gate=db_gate, norm_ffn_g=dg_ffn, norm_final_g=dg_final)
    return loss, dx, gw, small


def _position():
    return lax.axis_index("x"), lax.axis_index("y"), lax.axis_index("c")


def _other_chips(x, y):
    return [(1 - x, y), (x, 1 - y), (1 - x, 1 - y)]


ROW_TILE = 16


def _row_chunks(rows, want=4):
    n = want
    while n > 1 and rows % (n * ROW_TILE):
        n //= 2
    return n


SEM_SPEC = pl.BlockSpec(memory_space=pltpu.SEMAPHORE)
ANY_SPEC = pl.BlockSpec(memory_space=pl.ANY)
DATAFLOW = pltpu.SideEffectType.DATAFLOW_SIDE_EFFECTING


def _in_hbm(a):
    return pltpu.with_memory_space_constraint(a, pltpu.HBM)


def _split_copy_start(srcs, land_shapes, copies, after, *, name):
    n, m = len(srcs), len(land_shapes)

    def body(*refs):
        src_refs, land_refs = refs[:n], refs[n:n + m]
        send_sems, recv_sems = refs[n + m + 1], refs[n + m + 2]
        token = refs[-1]
        x, y, c = _position()
        for k, (src, dst, peer) in enumerate(copies(x, y, c, src_refs, land_refs)):
            pltpu.make_async_remote_copy(src_ref=src, dst_ref=dst, send_sem=send_sems.at[k], recv_sem=recv_sems.at[k],
                                         device_id=peer, device_id_type=MESH).start()
        token[...] = jnp.zeros_like(token)

    lands = [lax.empty(s.shape, s.dtype) for s in land_shapes]
    count = len(copies(0, 0, 0, srcs, lands))
    out = _pcall(
        body, name=name,
        out_shape=(pltpu.SemaphoreType.DMA((count,)), pltpu.SemaphoreType.DMA((count,)),
                   *[pltpu.HBM(s.shape, s.dtype) for s in srcs], *[pltpu.HBM(s.shape, s.dtype) for s in land_shapes],
                   jax.ShapeDtypeStruct((8, 128), F32)),
        in_specs=[HBM_SPEC] * (n + m) + [ANY_SPEC],
        out_specs=(SEM_SPEC, SEM_SPEC, *[HBM_SPEC] * (n + m), pl.BlockSpec(memory_space=pltpu.VMEM)),
        input_output_aliases={k: 2 + k for k in range(n + m)},
        compiler_params=pltpu.CompilerParams(has_side_effects=DATAFLOW),
    )(*[_in_hbm(s) for s in srcs], *[_in_hbm(l) for l in lands], after)
    return out[0], out[1], list(out[2:2 + n]), list(out[2 + n:2 + n + m]), out[-1]


def _split_copy_wait(send_sems, recv_sems, srcs, lands, copies, after, *, name):
    n, m = len(srcs), len(lands)

    def body(*refs):
        src_refs, land_refs = refs[:n], refs[n:n + m]
        send, recv = refs[n + m], refs[n + m + 1]
        x, y, c = _position()
        for k, (src, dst, peer) in enumerate(copies(x, y, c, src_refs, land_refs)):
            cp = pltpu.make_async_remote_copy(src_ref=src, dst_ref=dst, send_sem=send.at[k], recv_sem=recv.at[k],
                                              device_id=peer, device_id_type=MESH)
            cp.wait_send()
            cp.wait_recv()

    afters = list(after) if isinstance(after, (list, tuple)) else [after]
    out = _pcall(
        body, name=name,
        out_shape=tuple(pltpu.HBM(s.shape, s.dtype) for s in list(srcs) + list(lands)),
        in_specs=[HBM_SPEC] * (n + m) + [SEM_SPEC, SEM_SPEC] + [ANY_SPEC] * len(afters),
        out_specs=tuple([HBM_SPEC] * (n + m)),
        input_output_aliases={k: k for k in range(n + m)},
        compiler_params=pltpu.CompilerParams(has_side_effects=DATAFLOW),
    )(*srcs, *lands, send_sems, recv_sems, *afters)
    return list(out[:n]), list(out[n:])


def _gather_copies(x, y, c, shard_refs, land_refs):
    out = []
    for s, l in zip(shard_refs, land_refs):
        half = s.shape[0] // 2
        nq = _row_chunks(half)
        for cx, cy in _other_chips(x, y):
            for q in range(nq):
                rows = pl.ds(c * half + q * (half // nq), half // nq)
                out.append((s.at[rows, :], l.at[2 * x + y, rows, :], (cx, cy, c)))
    return out


def _scatter_copies(x, y, c, part_refs, land_refs):
    out = []
    for p, l in zip(part_refs, land_refs):
        nq = _row_chunks(p.shape[1])
        for r, (cx, cy) in enumerate(_other_chips(x, y)):
            for q in range(nq):
                rows = pl.ds(q * (p.shape[1] // nq), p.shape[1] // nq)
                out.append((p.at[2 * cx + cy, rows, :], l.at[r, rows, :], (cx, cy, c)))
    return out


def _forward_halves(lands, *, name):
    n = len(lands)

    def body(*refs):
        ins = refs[:n]
        send_sems, recv_sems = refs[2 * n:]
        x, y, c = _position()
        copies = []
        for w in range(n):
            half = ins[w].shape[1] // 2
            for r, (cx, cy) in enumerate(_other_chips(x, y)):
                blk = ins[w].at[2 * cx + cy, pl.ds(c * half, half), :]
                cp = pltpu.make_async_remote_copy(src_ref=blk, dst_ref=blk, send_sem=send_sems.at[w, r],
                                                  recv_sem=recv_sems.at[w, r], device_id=(x, y, 1 - c),
                                                  device_id_type=MESH)
                cp.start()
                copies.append(cp)
        for w in range(n):
            half = ins[w].shape[1] // 2
            for r, (cx, cy) in enumerate(_other_chips(x, y)):
                blk = ins[w].at[2 * cx + cy, pl.ds((1 - c) * half, half), :]
                pltpu.make_async_remote_copy(src_ref=blk, dst_ref=blk, send_sem=send_sems.at[w, r],
                                             recv_sem=recv_sems.at[w, r], device_id=(x, y, 1 - c),
                                             device_id_type=MESH).wait_recv()
        for cp in copies:
            cp.wait_send()

    return _pcall(
        body, name=name, in_specs=[HBM_SPEC] * n, out_specs=[HBM_SPEC] * n,
        out_shape=[jax.ShapeDtypeStruct(l.shape, l.dtype) for l in lands],
        input_output_aliases={k: k for k in range(n)},
        scratch_shapes=[pltpu.SemaphoreType.DMA((n, 3)), pltpu.SemaphoreType.DMA((n, 3))],
    )(*lands)


def _swap_halves(grads, name="swap_halves"):
    n = len(grads)

    def body(*refs):
        ins, outs = refs[:n], refs[n:2 * n]
        send_sems, recv_sems = refs[2 * n:]
        x, y, c = _position()
        copies = []
        for w in range(n):
            half = ins[w].shape[1] // 2
            cp = pltpu.make_async_remote_copy(
                src_ref=ins[w].at[:, pl.ds((1 - c) * half, half), :], dst_ref=outs[w], send_sem=send_sems.at[w],
                recv_sem=recv_sems.at[w], device_id=(x, y, 1 - c), device_id_type=MESH)
            cp.start()
            copies.append(cp)
        for cp in copies:
            cp.wait()

    return _pcall(
        body, name=name, in_specs=[HBM_SPEC] * n, out_specs=[HBM_SPEC] * n,
        out_shape=[jax.ShapeDtypeStruct((4, g.shape[1] // 2, g.shape[2]), g.dtype) for g in grads],
        scratch_shapes=[pltpu.SemaphoreType.DMA((n,)), pltpu.SemaphoreType.DMA((n,))],
    )(*grads)


def _share_halves(halves):
    n = len(halves)

    def body(*refs):
        ins, outs = refs[:n], refs[n:2 * n]
        send_sems, recv_sems = refs[2 * n:]
        x, y, c = _position()
        copies = []
        for w in range(n):
            cp = pltpu.make_async_remote_copy(src_ref=ins[w], dst_ref=outs[w], send_sem=send_sems.at[w],
                                              recv_sem=recv_sems.at[w], device_id=(x, y, 1 - c), device_id_type=MESH)
            cp.start()
            copies.append(cp)
        for cp in copies:
            cp.wait()

    return _pcall(
        body, name="share_halves", in_specs=[HBM_SPEC] * n, out_specs=[HBM_SPEC] * n,
        out_shape=[jax.ShapeDtypeStruct(h.shape, h.dtype) for h in halves],
        scratch_shapes=[pltpu.SemaphoreType.DMA((n,)), pltpu.SemaphoreType.DMA((n,))],
    )(*halves)


def _sum_small(part):
    rows, width = part.shape

    def body(x_ref, out_ref, all_ref, send_sems, recv_sems):
        x, y, c = _position()
        me, sibling = (x, y, c), (x, y, 1 - c)
        chips = _other_chips(x, y)

        def block(px, py, pc):
            return all_ref.at[pl.ds((4 * px + 2 * py + pc) * rows, rows), :]

        def copy(k, blk, to, src=None):
            return pltpu.make_async_remote_copy(
                src_ref=block(*blk) if src is None else src, dst_ref=block(*blk), send_sem=send_sems.at[k],
                recv_sem=recv_sems.at[k], device_id=to, device_id_type=MESH)

        all_ref[pl.ds((4 * x + 2 * y + c) * rows, rows), :] = x_ref[...]
        first = [copy(0, me, sibling, src=x_ref)]
        first += [copy(1 + j, me, (*chip, c), src=x_ref) for j, chip in enumerate(chips)]
        for cp in first:
            cp.start()
        passed = [copy(4 + j, (*chip, c), sibling) for j, chip in enumerate(chips)]
        for j, chip in enumerate(chips):
            copy(1 + j, (*chip, c), me).wait_recv()
            passed[j].start()
        copy(0, sibling, me).wait_recv()
        for j, chip in enumerate(chips):
            copy(4 + j, (*chip, 1 - c), me).wait_recv()
        for cp in first + passed:
            cp.wait_send()
        total = all_ref[0:rows, :]
        for d in range(1, 8):
            total = total + all_ref[d * rows:(d + 1) * rows, :]
        out_ref[...] = total

    vm = pl.BlockSpec(memory_space=pltpu.VMEM)
    return _pcall(
        body, name="sum_small", in_specs=[vm], out_specs=vm, out_shape=jax.ShapeDtypeStruct((rows, width), F32),
        scratch_shapes=[pltpu.VMEM((8 * rows, width), F32), pltpu.SemaphoreType.DMA((7,)), pltpu.SemaphoreType.DMA((7,))],
    )(part)


def _row_tile(R, C, itemsize=4, budget=1 << 20):
    for t in (512, 256, 128, 64, 32, 16, 8):
        if R % t == 0 and t * C * itemsize <= budget:
            return t
    return R


def _add_halves(g, recv, c, *, name):
    _, R, C = g.shape
    half = R // 2
    t = _row_tile(half, C)
    nb = half // t

    def body(c_ref, g_ref, r_ref, o_ref):
        o_ref[...] = (g_ref[...].astype(F32) + r_ref[...].astype(F32)).astype(o_ref.dtype)

    grid_spec = pltpu.PrefetchScalarGridSpec(
        num_scalar_prefetch=1, grid=(4, nb),
        in_specs=[pl.BlockSpec((1, t, C), lambda k, i, cr: (k, cr[0] * nb + i, 0)),
                  pl.BlockSpec((1, t, C), lambda k, i, cr: (k, i, 0))],
        out_specs=pl.BlockSpec((1, t, C), lambda k, i, cr: (k, i, 0)))
    return _pcall(body, name=name, grid_spec=grid_spec, out_shape=jax.ShapeDtypeStruct((4, half, C), g.dtype),
                  compiler_params=_params("parallel", "parallel"))(c, g, recv)


def _add_owners(mine, recv, *, name):
    half, C = mine.shape
    t = _row_tile(half, C)

    def body(m_ref, r_ref, o_ref):
        o_ref[...] = ((m_ref[...].astype(F32) + r_ref[0].astype(F32)) + r_ref[1].astype(F32)) + r_ref[2].astype(F32)

    return _pcall(body, name=name, grid=(half // t,),
                  in_specs=[pl.BlockSpec((t, C), lambda i: (i, 0)), pl.BlockSpec((3, t, C), lambda i: (0, i, 0))],
                  out_specs=pl.BlockSpec((t, C), lambda i: (i, 0)), out_shape=jax.ShapeDtypeStruct((half, C), F32),
                  compiler_params=_params("parallel"))(mine, recv)


def _adamw(w, g, m, v, *, name):
    R, C = w.shape
    t = _row_tile(R, C)
    c1 = 1.0 - ADAM_B1 ** ADAM_STEP
    c2 = 1.0 - ADAM_B2 ** ADAM_STEP

    def body(w_ref, g_ref, m_ref, v_ref, d_ref, nm_ref, nv_ref):
        gv = g_ref[...]
        mn = ADAM_B1 * m_ref[...] + (1.0 - ADAM_B1) * gv
        vn = ADAM_B2 * v_ref[...] + (1.0 - ADAM_B2) * (gv * gv)
        d_ref[...] = -ADAM_LR * ((mn / c1) / (jnp.sqrt(vn / c2) + ADAM_EPS) + ADAM_WD * w_ref[...])
        nm_ref[...] = mn
        nv_ref[...] = vn

    blk = pl.BlockSpec((t, C), lambda i: (i, 0))
    shp = jax.ShapeDtypeStruct((R, C), F32)
    return _pcall(body, name=name, grid=(R // t,), in_specs=[blk] * 4, out_specs=[blk] * 3, out_shape=[shp] * 3,
                  compiler_params=_params("parallel"))(w, g, m, v)


BIG = ("w_in", "w_dil_out", "w_fox_out", "w_out", "w_ffn_in", "w_ffn_down")
SMALL = ("norm_mix_g", "b_fgt", "b_gate", "norm_ffn_g", "norm_final_g")
ORDER = ("norm_mix_g", "w_in", "b_fgt", "b_gate", "w_dil_out", "w_fox_out", "w_out", "norm_ffn_g", "w_ffn_in",
         "w_ffn_down", "norm_final_g")
SMALL_ROWS = {"norm_mix_g": (0, 1), "b_gate": (1, 3), "norm_ffn_g": (3, 4), "norm_final_g": (4, 5), "b_fgt": (5, 6)}


def _columns_to_blocks(full, ncol):
    K = full.shape[0]
    return full.reshape(K, 4, ncol).transpose(1, 0, 2)


def _blocks_to_columns(blocks):
    n, K, ncol = blocks.shape
    return blocks.transpose(1, 0, 2).reshape(K, n * ncol)


def kernel(x, norm_mix_g, w_in, b_fgt, b_gate, w_dil_out, w_fox_out, w_out, norm_ffn_g, w_ffn_in, w_ffn_down, norm_final_g, loss_target, m_norm_mix_g, m_w_in, m_b_fgt, m_b_gate, m_w_dil_out, m_w_fox_out, m_w_out, m_norm_ffn_g, m_w_ffn_in, m_w_ffn_down, m_norm_final_g, v_norm_mix_g, v_w_in, v_b_fgt, v_b_gate, v_w_dil_out, v_w_fox_out, v_w_out, v_norm_ffn_g, v_w_ffn_in, v_w_ffn_down, v_norm_final_g):
    weights = dict(norm_mix_g=norm_mix_g, w_in=w_in, b_fgt=b_fgt, b_gate=b_gate, w_dil_out=w_dil_out,
                   w_fox_out=w_fox_out, w_out=w_out, norm_ffn_g=norm_ffn_g, w_ffn_in=w_ffn_in, w_ffn_down=w_ffn_down,
                   norm_final_g=norm_final_g)
    m_in = dict(norm_mix_g=m_norm_mix_g, w_in=m_w_in, b_fgt=m_b_fgt, b_gate=m_b_gate, w_dil_out=m_w_dil_out,
                w_fox_out=m_w_fox_out, w_out=m_w_out, norm_ffn_g=m_norm_ffn_g, w_ffn_in=m_w_ffn_in,
                w_ffn_down=m_w_ffn_down, norm_final_g=m_norm_final_g)
    v_in = dict(norm_mix_g=v_norm_mix_g, w_in=v_w_in, b_fgt=v_b_fgt, b_gate=v_b_gate, w_dil_out=v_w_dil_out,
                w_fox_out=v_w_fox_out, w_out=v_w_out, norm_ffn_g=v_norm_ffn_g, w_ffn_in=v_w_ffn_in,
                w_ffn_down=v_w_ffn_down, norm_final_g=v_norm_final_g)
    c = lax.axis_index("c")
    chip = 2 * lax.axis_index("x") + lax.axis_index("y")

    shards = {n: weights[n][0].astype(_CD) for n in BIG}
    in_shape = jax.ShapeDtypeStruct((4,) + shards["w_in"].shape, _CD)
    send_i, recv_i, in_src, in_land, token_in = _split_copy_start(
        [shards["w_in"]], [in_shape], _gather_copies, norm_mix_g, name="gather_in_start")
    late = BIG[1:]
    send_g, recv_g, late_src, late_land, token = _split_copy_start(
        [shards[n] for n in late], [jax.ShapeDtypeStruct((4,) + shards[n].shape, _CD) for n in late],
        _gather_copies, token_in, name="gather_late_start")
    adam_in = [t[0] + token_in[0, 0] for t in (w_in, m_w_in, v_w_in)]
    p = dict(norm_mix_g=norm_mix_g, b_fgt=jnp.pad(b_fgt, ((0, 0), (0, F_PAD - N_FOX_HEADS))), b_gate=b_gate,
             norm_ffn_g=norm_ffn_g, norm_final_g=norm_final_g.reshape(1, D_MODEL))

    def first_weights(after):
        own, lands = _split_copy_wait(send_i, recv_i, in_src, in_land, _gather_copies, [after] + adam_in,
                                      name="gather_in_wait")
        (g_in,) = _forward_halves(lands, name="gather_in_forward")
        full_in = _blocks_to_columns(lax.dynamic_update_index_in_dim(g_in, own[0], chip, 0))
        o3 = QKV_COLS
        o4 = o3 + N_FOX_HEADS
        return dict(qkv=full_in[:, :o3], f=jnp.pad(full_in[:, o3:o4], ((0, 0), (0, F_PAD - N_FOX_HEADS))),
                    g=full_in[:, o4:])

    def late_weights(after):
        own, lands = _split_copy_wait(send_g, recv_g, late_src, late_land, _gather_copies, after,
                                      name="gather_late_wait")
        lands = _forward_halves(lands, name="gather_late_forward")
        g_dil, g_fox, g_out, g_ffn_in, g_ffn_down = [
            lax.dynamic_update_index_in_dim(l, s, chip, 0) for l, s in zip(lands, own)]
        return dict(dil_out=_blocks_to_columns(g_dil), fox_out=_blocks_to_columns(g_fox),
                    out=g_out.reshape(D_MODEL, D_MODEL), ffn_in=g_ffn_in,
                    ffn_down=g_ffn_down.reshape(D_FF, D_MODEL))

    c_arr = jnp.reshape(c, (1,)).astype(jnp.int32)

    def to_blocks(n, full):
        shape = weights[n].shape
        if full.ndim == 3:
            return full
        if n in ("w_out", "w_ffn_down"):
            return full.reshape(4, shape[1], shape[2])
        return _columns_to_blocks(full, shape[2])

    def pair_sums(group, named):
        names = list(named)
        blocks = [to_blocks(n, named[n]) for n in names]
        from_sibling = _swap_halves(blocks, name=f"swap_halves_{group}")
        return [_add_halves(b, r, c_arr, name=f"add_halves_{n}") for b, r, n in zip(blocks, from_sibling, names)]

    in_flight = {}

    def grad_sink(group, gw):
        if group == "in":
            named = {"w_in": jnp.concatenate([gw["qkv"], gw["f"][:, :N_FOX_HEADS], gw["g"]], axis=1)}
        else:
            named = {"w_" + k: v for k, v in gw.items()}
        sums = pair_sums(group, named)
        started = _split_copy_start(sums, [jax.ShapeDtypeStruct((3,) + s.shape[1:], s.dtype) for s in sums],
                                    _scatter_copies, next(iter(gw.values())), name=f"scatter_{group}_start")
        in_flight[group] = (list(named), started)
        return started[-1]

    loss_part, grad_x, gw, small = _layer_step(x[0], loss_target[0], {}, p, late_weights, grad_sink,
                                               (token_in, token), first_weights)

    def owner_sums(names, sums, from_chips):
        return {n: _add_owners(lax.dynamic_index_in_dim(s, chip, 0, keepdims=False), r, name=f"add_owners_{n}")
                for n, s, r in zip(names, sums, from_chips)}

    halves = {}
    for group, (names, (send_s, recv_s, srcs, lands, _)) in in_flight.items():
        sums, from_chips = _split_copy_wait(send_s, recv_s, srcs, lands, _scatter_copies, grad_x,
                                            name=f"scatter_{group}_wait")
        halves.update(owner_sums(names, sums, from_chips))
    halves = [halves[n] for n in BIG]
    grads = {}
    for n, own, other in zip(BIG, halves, _share_halves(halves)):
        pair = jnp.stack([own, other])
        grads[n] = jnp.where(c == 0, pair, pair[::-1]).reshape(2 * own.shape[0], own.shape[1])

    packed = jnp.concatenate([
        small["norm_mix_g"], small["b_gate"].reshape(2, D_MODEL), small["norm_ffn_g"], small["norm_final_g"],
        jnp.pad(small["b_fgt"], ((0, 0), (0, D_MODEL - F_PAD))), jnp.pad(loss_part, ((0, 0), (0, D_MODEL - 1))),
        jnp.zeros((1, D_MODEL), F32)], axis=0)
    summed = _sum_small(packed)
    for n in SMALL:
        lo, hi = SMALL_ROWS[n]
        grads[n] = summed[lo:hi].reshape(1, -1)[:, :weights[n].size]
    loss = summed[6, 0]

    out_g, out_d, out_m, out_v = {}, {}, {}, {}
    for n in ORDER:
        shape = weights[n].shape
        two_d = shape[1:] if len(shape) == 3 else (1, weights[n].size)
        g2 = grads[n].reshape(two_d)
        wmv = adam_in if n == "w_in" else [t.reshape(two_d) for t in (weights[n], m_in[n], v_in[n])]
        d2, m2, v2 = _adamw(wmv[0], g2, wmv[1], wmv[2], name=f"adamw_{n}")
        out_g[n], out_d[n], out_m[n], out_v[n] = (g2.reshape(shape), d2.reshape(shape), m2.reshape(shape),
                                                  v2.reshape(shape))
    return (loss, grad_x[None], *[out_g[n] for n in ORDER], *[out_d[n] for n in ORDER],
            *[out_m[n] for n in ORDER], *[out_v[n] for n in ORDER])
```

```python
import numpy as np
import jax
import jax.numpy as jnp
from jax import lax
from jax.experimental import pallas as pl
from jax.experimental.pallas import tpu as pltpu

F32 = jnp.float32
_CD = jnp.bfloat16

D_MODEL = 1024
HEAD_DIM = 64
DIL_PAIRS = ((128, 1), (512, 4), (2048, 16))
N_DIL_GROUPS = 3
DIL_HEADS = 4
DIL_W = 128
DIL_OUT = DIL_HEADS * HEAD_DIM
DIL_WIDTH = N_DIL_GROUPS * DIL_OUT
N_FOX_HEADS = 8
FOX_WIDTH = N_FOX_HEADS * HEAD_DIM
D_FF = 2816
QKV_COLS = 3 * DIL_WIDTH + 3 * FOX_WIDTH
F_PAD = 128
RMS_EPS = 1e-6
NEG_INF = -1e30
ATTN_SCALE = HEAD_DIM ** -0.5
ADAM_LR, ADAM_B1, ADAM_B2, ADAM_EPS, ADAM_WD, ADAM_STEP = 0.001, 0.9, 0.999, 1e-08, 0.01, 10

VMEM_LIMIT = 48 * 1024 * 1024
VMEM_LIMIT_RESIDENT = 56 * 1024 * 1024
LANES = 128
MESH = pl.DeviceIdType.MESH
HBM_SPEC = pl.BlockSpec(memory_space=pltpu.HBM)


def _pcall(body, after=None, **kw):
    if after is None:
        return pl.pallas_call(body, **kw)
    n_in = len(kw["in_specs"])
    kw["in_specs"] = list(kw["in_specs"]) + [pl.BlockSpec(memory_space=pl.ANY)]

    def tied(*refs):
        return body(*refs[:n_in], *refs[n_in + 1:])

    call = pl.pallas_call(tied, **kw)
    return lambda *args: call(*args, after)


def _params(*sem):
    return pltpu.CompilerParams(dimension_semantics=sem, vmem_limit_bytes=VMEM_LIMIT)


def _pick(dim, pref):
    t = (min(pref, dim) // 128) * 128
    while t >= 128:
        if dim % t == 0:
            return t
        t -= 128
    return dim


def _mm(a, b, *, name, ta=False, tb=False, out_dtype=F32, add=None, tm=1024, tn=512, tk=2048, after=None,
        b_blocks=False, out_blocks=None, a_halves=False, b_halves=False):
    if a_halves:
        M, K = a.shape[1], 2 * a.shape[2]
    elif ta:
        K, M = a.shape
    else:
        M, K = a.shape
    if b_halves:
        b_rows, b_cols = b.shape[1], 2 * b.shape[2]
    else:
        b_rows, b_cols = (b.shape[1], b.shape[0] * b.shape[2]) if b_blocks else b.shape
    if tb:
        N, K2 = b_rows, b_cols
    else:
        K2, N = b_rows, b_cols
    assert K == K2, (a.shape, b.shape)
    shard = b.shape[2] if b_blocks else None
    tm = _pick(M, tm)
    tn = _pick(shard if (b_blocks and not tb) else (out_blocks or N), tn)
    tk = _pick(shard if (b_blocks and tb) else K, tk)
    nk = K // tk
    dn = (((0 if ta else 1,), (1 if tb else 0,)), ((), ()))
    has_add = add is not None
    assert not (has_add and out_blocks)

    def body(*refs):
        a_ref, b_ref = refs[0], refs[1]
        add_ref = refs[2] if has_add else None
        o_ref = refs[3] if has_add else refs[2]
        bv = b_ref[0] if b_blocks else b_ref[...]
        p = lax.dot_general(a_ref[...].astype(_CD), bv.astype(_CD), dn, preferred_element_type=F32)

        def finish(r):
            if has_add:
                r = r + add_ref[...]
            if out_blocks:
                o_ref[0] = r.astype(out_dtype)
            else:
                o_ref[...] = r.astype(out_dtype)

        if nk == 1:
            finish(p)
        else:
            acc_ref = refs[-1]
            k = pl.program_id(2)

            @pl.when(k == 0)
            def _():
                acc_ref[...] = p

            @pl.when(k > 0)
            def _():
                acc_ref[...] += p

            @pl.when(k == nk - 1)
            def _():
                finish(acc_ref[...])

    if a_halves:
        ka = (K // 2) // tk
        a_spec = pl.BlockSpec((None, tm, tk), lambda i, j, k: (k // ka, i, k % ka))
    else:
        a_spec = pl.BlockSpec((tk, tm), lambda i, j, k: (k, i)) if ta else pl.BlockSpec((tm, tk), lambda i, j, k: (i, k))
    if b_halves:
        nb_ = (N // 2) // tn
        b_spec = pl.BlockSpec((None, tk, tn), lambda i, j, k: (j // nb_, k, j % nb_))
    elif b_blocks and tb:
        per = shard // tk
        b_spec = pl.BlockSpec((1, tn, tk), lambda i, j, k: (k // per, j, k % per))
    elif b_blocks:
        per = shard // tn
        b_spec = pl.BlockSpec((1, tk, tn), lambda i, j, k: (j // per, k, j % per))
    else:
        b_spec = pl.BlockSpec((tn, tk), lambda i, j, k: (j, k)) if tb else pl.BlockSpec((tk, tn), lambda i, j, k: (k, j))
    if out_blocks:
        oper = out_blocks // tn
        o_spec = pl.BlockSpec((1, tm, tn), lambda i, j, k: (j // oper, i, j % oper))
        out_shape = jax.ShapeDtypeStruct((N // out_blocks, M, out_blocks), out_dtype)
    else:
        o_spec = pl.BlockSpec((tm, tn), lambda i, j, k: (i, j))
        out_shape = jax.ShapeDtypeStruct((M, N), out_dtype)
    in_specs = [a_spec, b_spec] + ([o_spec] if has_add else [])
    args = (a, b) + ((add,) if has_add else ())
    return _pcall(
        body, after, name=name, grid=(M // tm, N // tn, nk), in_specs=in_specs, out_specs=o_spec,
        out_shape=out_shape,
        scratch_shapes=[pltpu.VMEM((tm, tn), F32)] if nk > 1 else [],
        compiler_params=_params("parallel", "parallel", "arbitrary"),
    )(*args)


def _rms_fwd(x, g, *, name, tm=512, after=None):
    S, D = x.shape

    def body(x_ref, g_ref, h_ref):
        xv = x_ref[...]
        r = lax.rsqrt(jnp.mean(xv * xv, axis=-1, keepdims=True) + RMS_EPS)
        h_ref[...] = ((xv * r) * g_ref[...]).astype(h_ref.dtype)

    row = pl.BlockSpec((tm, D), lambda i: (i, 0))
    return _pcall(body, after, name=name, grid=(S // tm,), in_specs=[row, pl.BlockSpec((1, D), lambda i: (0, 0))],
                  out_specs=row, out_shape=jax.ShapeDtypeStruct((S, D), _CD), compiler_params=_params("parallel"))(x, g)


def _rms_bwd(x, g, dh, dres, *, name, tm=512, after=None):
    S, D = x.shape

    def body(x_ref, g_ref, dh_ref, dres_ref, dx_ref, dg_ref):
        xv = x_ref[...]
        r = lax.rsqrt(jnp.mean(xv * xv, axis=-1, keepdims=True) + RMS_EPS)
        xh = xv * r
        dhv = dh_ref[...]
        dxh = dhv * g_ref[...]
        dx_ref[...] = dres_ref[...] + r * (dxh - xh * jnp.mean(dxh * xh, axis=-1, keepdims=True))
        part = jnp.sum(dhv * xh, axis=0, keepdims=True)

        @pl.when(pl.program_id(0) == 0)
        def _():
            dg_ref[...] = part

        @pl.when(pl.program_id(0) > 0)
        def _():
            dg_ref[...] += part

    row = pl.BlockSpec((tm, D), lambda i: (i, 0))
    vec = pl.BlockSpec((1, D), lambda i: (0, 0))
    return _pcall(body, after, name=name, grid=(S // tm,), in_specs=[row, vec, row, row], out_specs=[row, vec],
                  out_shape=[jax.ShapeDtypeStruct((S, D), F32), jax.ShapeDtypeStruct((1, D), F32)],
                  compiler_params=_params("arbitrary"))(x, g, dh, dres)


def _loss_head(x, g, tgt, *, name, tm=512):
    S, D = x.shape

    def body(x_ref, g_ref, t_ref, loss_ref, dx_ref, dg_ref):
        xv = x_ref[...]
        gv = g_ref[...]
        r = lax.rsqrt(jnp.mean(xv * xv, axis=-1, keepdims=True) + RMS_EPS)
        xh = xv * r
        err = xh * gv - t_ref[...]
        lpart = 0.5 * jnp.sum(jnp.mean(err * err, axis=-1, keepdims=True), axis=0, keepdims=True)
        dy = err * (1.0 / D)
        dxh = dy * gv
        dx_ref[...] = r * (dxh - xh * jnp.mean(dxh * xh, axis=-1, keepdims=True))
        gpart = jnp.sum(dy * xh, axis=0, keepdims=True)

        @pl.when(pl.program_id(0) == 0)
        def _():
            loss_ref[...] = lpart
            dg_ref[...] = gpart

        @pl.when(pl.program_id(0) > 0)
        def _():
            loss_ref[...] += lpart
            dg_ref[...] += gpart

    row = pl.BlockSpec((tm, D), lambda i: (i, 0))
    vec = pl.BlockSpec((1, D), lambda i: (0, 0))
    one = pl.BlockSpec((1, 1), lambda i: (0, 0))
    return _pcall(body, name=name, grid=(S // tm,), in_specs=[row, vec, row], out_specs=[one, row, vec],
                  out_shape=[jax.ShapeDtypeStruct((1, 1), F32), jax.ShapeDtypeStruct((S, D), F32),
                             jax.ShapeDtypeStruct((1, D), F32)],
                  compiler_params=_params("arbitrary"))(x, g, tgt)


def _sigmoid(z):
    return 1.0 / (1.0 + jnp.exp(-z))


def _gate_fwd(gl, bg, ya, yb, *, name, tm=512):
    S, D = ya.shape

    def body(za_ref, zb_ref, ba_ref, bb_ref, ya_ref, yb_ref, o_ref):
        ga = _sigmoid(za_ref[...].astype(F32) + ba_ref[...])
        gb = _sigmoid(zb_ref[...].astype(F32) + bb_ref[...])
        o_ref[...] = (ga * ya_ref[...].astype(F32) + gb * yb_ref[...].astype(F32)).astype(o_ref.dtype)

    lo = pl.BlockSpec((tm, D), lambda i: (i, 0))
    hi = pl.BlockSpec((tm, D), lambda i: (i, 1))
    vlo = pl.BlockSpec((1, D), lambda i: (0, 0))
    vhi = pl.BlockSpec((1, D), lambda i: (0, 1))
    return _pcall(body, name=name, grid=(S // tm,), in_specs=[lo, hi, vlo, vhi, lo, lo], out_specs=lo,
                  out_shape=jax.ShapeDtypeStruct((S, D), _CD), compiler_params=_params("parallel"))(gl, gl, bg, bg, ya, yb)


def _gate_bwd(dm, gl, bg, ya, yb, *, name, tm=256):
    S, D = ya.shape

    def body(dm_ref, za_ref, zb_ref, ba_ref, bb_ref, ya_ref, yb_ref, dya_ref, dyb_ref, dgl_ref, dbg_ref):
        dmv = dm_ref[...].astype(F32)
        ga = _sigmoid(za_ref[...].astype(F32) + ba_ref[...])
        gb = _sigmoid(zb_ref[...].astype(F32) + bb_ref[...])
        dya_ref[...] = (dmv * ga).astype(dya_ref.dtype)
        dyb_ref[...] = (dmv * gb).astype(dyb_ref.dtype)
        dza = dmv * ya_ref[...].astype(F32) * ga * (1.0 - ga)
        dzb = dmv * yb_ref[...].astype(F32) * gb * (1.0 - gb)
        dgl_ref[:, :D] = dza.astype(dgl_ref.dtype)
        dgl_ref[:, D:] = dzb.astype(dgl_ref.dtype)
        pa = jnp.sum(dza, axis=0, keepdims=True)
        pb = jnp.sum(dzb, axis=0, keepdims=True)

        @pl.when(pl.program_id(0) == 0)
        def _():
            dbg_ref[:, :D] = pa
            dbg_ref[:, D:] = pb

        @pl.when(pl.program_id(0) > 0)
        def _():
            dbg_ref[:, :D] += pa
            dbg_ref[:, D:] += pb

    lo = pl.BlockSpec((tm, D), lambda i: (i, 0))
    hi = pl.BlockSpec((tm, D), lambda i: (i, 1))
    vlo = pl.BlockSpec((1, D), lambda i: (0, 0))
    vhi = pl.BlockSpec((1, D), lambda i: (0, 1))
    wide = pl.BlockSpec((tm, 2 * D), lambda i: (i, 0))
    vwide = pl.BlockSpec((1, 2 * D), lambda i: (0, 0))
    return _pcall(body, name=name, grid=(S // tm,), in_specs=[lo, lo, hi, vlo, vhi, lo, lo],
                  out_specs=[lo, lo, wide, vwide],
                  out_shape=[jax.ShapeDtypeStruct((S, D), _CD), jax.ShapeDtypeStruct((S, D), _CD),
                             jax.ShapeDtypeStruct((S, 2 * D), _CD), jax.ShapeDtypeStruct((1, 2 * D), F32)],
                  compiler_params=_params("arbitrary"))(dm, gl, gl, bg, bg, ya, yb)


def _ffn_in_act(h2, w_blocks, *, name, tm=512):
    S, D = h2.shape
    _, _, C = w_blocks.shape

    def body(a_ref, bg_ref, bu_ref, g_ref, u_ref, o_ref):
        av = a_ref[...].astype(_CD)
        gv = jnp.dot(av, bg_ref[0].astype(_CD), preferred_element_type=F32)
        uv = jnp.dot(av, bu_ref[0].astype(_CD), preferred_element_type=F32)
        g_ref[...] = gv.astype(g_ref.dtype)
        u_ref[...] = uv.astype(u_ref.dtype)
        o_ref[...] = (gv * _sigmoid(gv) * uv).astype(o_ref.dtype)

    out = pl.BlockSpec((tm, C), lambda i, j: (i, j))
    shp = jax.ShapeDtypeStruct((S, 2 * C), _CD)
    return _pcall(body, name=name, grid=(S // tm, 2),
                  in_specs=[pl.BlockSpec((tm, D), lambda i, j: (i, 0)), pl.BlockSpec((1, D, C), lambda i, j: (j, 0, 0)),
                            pl.BlockSpec((1, D, C), lambda i, j: (2 + j, 0, 0))],
                  out_specs=[out, out, out], out_shape=[shp, shp, shp],
                  compiler_params=_params("parallel", "arbitrary"))(h2, w_blocks, w_blocks)


def _d_swiglu(dx, w_down, gate, up, *, name, tm=512, tn=1408):
    S, D = dx.shape
    F = w_down.shape[0]
    nt = (((1,), (1,)), ((), ()))

    def body(a_ref, b_ref, g_ref, u_ref, o_ref):
        dv = lax.dot_general(a_ref[...].astype(_CD), b_ref[...].astype(_CD), nt, preferred_element_type=F32)
        gv = g_ref[...].astype(F32)
        sg = _sigmoid(gv)
        o_ref[0] = (dv * u_ref[...].astype(F32) * (sg * (1.0 + gv * (1.0 - sg)))).astype(o_ref.dtype)
        o_ref[1] = (dv * (gv * sg)).astype(o_ref.dtype)

    tile = pl.BlockSpec((tm, tn), lambda i, j: (i, j))
    return _pcall(body, name=name, grid=(S // tm, F // tn),
                  in_specs=[pl.BlockSpec((tm, D), lambda i, j: (i, 0)), pl.BlockSpec((tn, D), lambda i, j: (j, 0)),
                            tile, tile],
                  out_specs=pl.BlockSpec((2, tm, tn), lambda i, j: (0, i, j)),
                  out_shape=jax.ShapeDtypeStruct((2, S, F), _CD),
                  compiler_params=_params("parallel", "arbitrary"))(dx, w_down, gate, up)


def _split3(x):
    hi = x.astype(jnp.bfloat16)
    r1 = x - hi.astype(F32)
    mid = r1.astype(jnp.bfloat16)
    lo = (r1 - mid.astype(F32)).astype(jnp.bfloat16)
    return hi, mid, lo


def _ones_dot_left(ones, x):
    return sum(jnp.dot(ones, p, preferred_element_type=F32) for p in _split3(x))


def _ones_dot_right(x, ones):
    return sum(jnp.dot(p, ones, preferred_element_type=F32) for p in _split3(x))


def _head_sum(x):
    n = x.shape[1]
    r = lax.broadcasted_iota(jnp.int32, (n, n), 0) // HEAD_DIM
    c = lax.broadcasted_iota(jnp.int32, (n, n), 1) // HEAD_DIM
    return _ones_dot_right(x, (r == c).astype(jnp.bfloat16))


def _log_sigmoid(z):
    e = jnp.exp(-jnp.abs(z))
    t = 1.0 + e
    log1p_e = jnp.where(t == 1.0, e, jnp.log(t) * (e / jnp.where(t == 1.0, 1.0, t - 1.0)))
    return jnp.minimum(z, 0.0) - log1p_e


def _fox_cumsum(zf, bf, *, name):
    S, W = zf.shape
    nb = S // 128

    def body(z_ref, b_ref, c_ref):
        tri = (lax.broadcasted_iota(jnp.int32, (128, 128), 0) >= lax.broadcasted_iota(jnp.int32, (128, 128), 1))
        tri = tri.astype(jnp.bfloat16)

        def step(i, carry):
            rows = pl.ds(pl.multiple_of(i * 128, 128), 128)
            lf = _log_sigmoid(z_ref[rows, :] + b_ref[...])
            cb = _ones_dot_left(tri, lf) + carry
            c_ref[rows, :] = cb
            return cb[127:128, :]

        lax.fori_loop(0, nb, step, jnp.zeros((1, W), F32))

    return _pcall(body, name=name, out_shape=jax.ShapeDtypeStruct((S, W), F32),
                  compiler_params=pltpu.CompilerParams(vmem_limit_bytes=VMEM_LIMIT))(zf, bf)


def _fox_cumsum_bwd(dc, zf, bf, *, name):
    S, W = zf.shape
    nb = S // 128

    def body(dc_ref, z_ref, b_ref, dz_ref, db_ref):
        tri = (lax.broadcasted_iota(jnp.int32, (128, 128), 0) <= lax.broadcasted_iota(jnp.int32, (128, 128), 1))
        tri = tri.astype(jnp.bfloat16)

        def step(k, carry):
            tail, acc = carry
            i = nb - 1 - k
            rows = pl.ds(pl.multiple_of(i * 128, 128), 128)
            dlf = _ones_dot_left(tri, dc_ref[rows, :]) + tail
            dz = dlf * _sigmoid(-(z_ref[rows, :] + b_ref[...]))
            dz_ref[rows, :] = dz
            return dlf[0:1, :], acc + jnp.sum(dz, axis=0, keepdims=True)

        _, acc = lax.fori_loop(0, nb, step, (jnp.zeros((1, W), F32), jnp.zeros((1, W), F32)))
        db_ref[...] = acc

    return _pcall(body, name=name,
                  out_shape=[jax.ShapeDtypeStruct((S, W), F32), jax.ShapeDtypeStruct((1, W), F32)],
                  compiler_params=pltpu.CompilerParams(vmem_limit_bytes=VMEM_LIMIT))(dc, zf, bf)


def _proj_dil(h, w_qkv, *, name, tm=1024):
    S, D = h.shape
    tn = DIL_WIDTH

    def body(a_ref, b_ref, *rest):
        outs, acc = rest[:N_DIL_GROUPS], rest[N_DIL_GROUPS]
        prod = jnp.dot(a_ref[...].astype(_CD), b_ref[...].astype(_CD), preferred_element_type=F32)
        for k in range(tn // LANES):
            acc[k] = prod[:, k * LANES:(k + 1) * LANES]
        for g, (_, d) in enumerate(DIL_PAIRS):
            for half in range(DIL_OUT // LANES):
                k = g * (DIL_OUT // LANES) + half
                cols = slice(half * LANES, (half + 1) * LANES)
                for r in range(d):
                    rows = pl.ds(r, tm // d, stride=d) if d > 1 else slice(None)
                    outs[g][0, r, :, cols] = acc[k, rows, :].astype(outs[g].dtype)

    out_specs = [pl.BlockSpec((1, d, tm // d, DIL_OUT), lambda i, j: (j, 0, i, 0)) for _, d in DIL_PAIRS]
    out_shape = [jax.ShapeDtypeStruct((3, d, S // d, DIL_OUT), _CD) for _, d in DIL_PAIRS]
    outs = _pcall(body, name=name, grid=(S // tm, 3),
                  in_specs=[pl.BlockSpec((tm, D), lambda i, j: (i, 0)), pl.BlockSpec((D, tn), lambda i, j: (0, j))],
                  out_specs=out_specs, out_shape=out_shape, scratch_shapes=[pltpu.VMEM((tn // LANES, tm, LANES), F32)],
                  compiler_params=_params("parallel", "arbitrary"))(h, w_qkv)
    return [o.reshape(3, S, DIL_OUT) for o in outs]


def _dil_start(block, S, dilation):
    sub = S // dilation
    u0 = block * DIL_W
    return (u0 % sub) * dilation + u0 // sub


def _dil_slopes(group):
    h = np.arange(1, N_DIL_GROUPS * DIL_HEADS + 1, dtype=np.float32)
    s = (np.float32(2.0) ** (np.float32(-8.0) * h / np.float32(N_DIL_GROUPS * DIL_HEADS))).astype(np.float32)
    return [float(v) for v in s.reshape(N_DIL_GROUPS, DIL_HEADS)[group]]


def _dil_tiles(i, n, blocks_per_seq):
    qi = lax.broadcasted_iota(jnp.int32, (DIL_W, 2 * DIL_W), 0)
    kj = lax.broadcasted_iota(jnp.int32, (DIL_W, 2 * DIL_W), 1)
    rel = qi + DIL_W - kj
    first = ((4 * n + i) % blocks_per_seq) == 0
    valid = jnp.logical_and(jnp.logical_and(rel >= 0, rel <= DIL_W), jnp.logical_or(kj >= DIL_W, jnp.logical_not(first)))
    return valid, rel.astype(F32)


def _dil_window(cur_ref, prev_ref, i, cols):
    if i > 0:
        return cur_ref[(i - 1) * DIL_W:(i + 1) * DIL_W, cols]
    return jnp.concatenate([prev_ref[:, cols], cur_ref[:DIL_W, cols]], axis=0)


CHUNK = 4 * DIL_W


def _dil_rows(block, S, dilation):
    start = _dil_start(block, S, dilation)
    return pl.ds(start, DIL_W, stride=dilation) if dilation > 1 else pl.ds(start, DIL_W)


def SPLIT(S):
    return (DIL_OUT // LANES, S, LANES)


def _dil_fwd(qkv, group, *, name):
    S = qkv.shape[1]
    dilation = DIL_PAIRS[group][1]
    bps = (S // dilation) // DIL_W
    slopes = _dil_slopes(group)
    nt = (((1,), (1,)), ((), ()))

    def body(q_ref, k_ref, v_ref, kp_ref, vp_ref, on_ref, ln_ref, o_ref, l_ref):
        n = pl.program_id(0)
        for i in range(4):
            valid, rel = _dil_tiles(i, n, bps)
            rows = slice(i * DIL_W, (i + 1) * DIL_W)
            for h in range(DIL_HEADS):
                cols = slice(h * HEAD_DIM, (h + 1) * HEAD_DIM)
                qh = q_ref[rows, cols]
                k2, v2 = _dil_window(k_ref, kp_ref, i, cols), _dil_window(v_ref, vp_ref, i, cols)
                s = lax.dot_general(qh, k2, nt, preferred_element_type=F32) * ATTN_SCALE - (slopes[h] * dilation) * rel
                s = jnp.where(valid, s, NEG_INF)
                m = jnp.max(s, axis=-1, keepdims=True)
                p = jnp.exp(s - m)
                den = jnp.sum(p, axis=-1, keepdims=True)
                acc = jnp.dot(p.astype(_CD), v2, preferred_element_type=F32)
                o_ref[rows, cols] = acc / den
                l_ref[rows, cols] = jnp.broadcast_to(m + jnp.log(den), (DIL_W, HEAD_DIM))
        for i in range(4):
            rows = slice(i * DIL_W, (i + 1) * DIL_W)
            nat = _dil_rows(4 * n + i, S, dilation)
            for half in range(DIL_OUT // LANES):
                cols = slice(half * LANES, (half + 1) * LANES)
                on_ref[half, nat, :] = o_ref[rows, cols]
                ln_ref[half, nat, :] = l_ref[rows, cols]

    def cur(which):
        return pl.BlockSpec((None, CHUNK, DIL_OUT), lambda n: (which, n, 0))

    def prev(which):
        return pl.BlockSpec((None, DIL_W, DIL_OUT), lambda n: (which, jnp.maximum(4 * n - 1, 0), 0))

    whole = pl.BlockSpec(SPLIT(S), lambda n: (0, 0, 0))
    return _pcall(body, name=name, grid=(S // CHUNK,), in_specs=[cur(0), cur(1), cur(2), prev(1), prev(2)],
                  out_specs=[whole, whole],
                  out_shape=[jax.ShapeDtypeStruct(SPLIT(S), F32), jax.ShapeDtypeStruct(SPLIT(S), F32)],
                  scratch_shapes=[pltpu.VMEM((CHUNK, DIL_OUT), F32), pltpu.VMEM((CHUNK, DIL_OUT), F32)],
                  compiler_params=_params("arbitrary"))(qkv, qkv, qkv, qkv, qkv)


STAT_OFFSET = HEAD_DIM // 2


def _dil_bwd(qkv, stats, do, group, *, name):
    S = qkv.shape[1]
    dilation = DIL_PAIRS[group][1]
    bps = (S // dilation) // DIL_W
    slopes = _dil_slopes(group)
    nchunk = S // CHUNK
    nt = (((1,), (1,)), ((), ()))
    tn = (((0,), (0,)), ((), ()))

    def body(q_ref, k_ref, v_ref, kp_ref, vp_ref, ln_ref, don_ref, dqn_ref, dkn_ref, dvn_ref,
             dk_s, dv_s, l_ref, do_ref, dq_ref):
        step = pl.program_id(0)
        n = nchunk - 1 - step
        for i in range(4):
            rows = slice(i * DIL_W, (i + 1) * DIL_W)
            nat = _dil_rows(4 * n + i, S, dilation)
            for half in range(DIL_OUT // LANES):
                cols = slice(half * LANES, (half + 1) * LANES)
                l_ref[rows, cols] = ln_ref[half, nat, :]
                do_ref[rows, cols] = don_ref[half, nat, :]

        @pl.when(step == 0)
        def _():
            dk_s[CHUNK:, :] = jnp.zeros((DIL_W, DIL_OUT), F32)
            dv_s[CHUNK:, :] = jnp.zeros((DIL_W, DIL_OUT), F32)

        dk_s[:CHUNK, :] = jnp.zeros((CHUNK, DIL_OUT), F32)
        dv_s[:CHUNK, :] = jnp.zeros((CHUNK, DIL_OUT), F32)
        for i in range(4):
            valid, rel = _dil_tiles(i, n, bps)
            rows = slice(i * DIL_W, (i + 1) * DIL_W)
            window = slice(i * DIL_W, (i + 2) * DIL_W)
            for h in range(DIL_HEADS):
                cols = slice(h * HEAD_DIM, (h + 1) * HEAD_DIM)
                qh = q_ref[rows, cols]
                k2, v2 = _dil_window(k_ref, kp_ref, i, cols), _dil_window(v_ref, vp_ref, i, cols)
                lh = l_ref[rows, h * HEAD_DIM:h * HEAD_DIM + 1]
                shift = l_ref[rows, h * HEAD_DIM + STAT_OFFSET:h * HEAD_DIM + STAT_OFFSET + 1]
                s = lax.dot_general(qh, k2, nt, preferred_element_type=F32) * ATTN_SCALE - (slopes[h] * dilation) * rel
                p = jnp.exp(jnp.where(valid, s, NEG_INF) - lh)
                dob = do_ref[rows, cols].astype(_CD)
                ds = p * (lax.dot_general(dob, v2, nt, preferred_element_type=F32) + shift)
                dsb = (ds * ATTN_SCALE).astype(_CD)
                dq_ref[rows, cols] = jnp.dot(dsb, k2, preferred_element_type=F32)
                dk_s[window, cols] += lax.dot_general(dsb, qh, tn, preferred_element_type=F32)
                dv_s[window, cols] += lax.dot_general(p.astype(_CD), dob, tn, preferred_element_type=F32)
        for i in range(4):
            rows = slice(i * DIL_W, (i + 1) * DIL_W)
            done = slice((i + 1) * DIL_W, (i + 2) * DIL_W)
            nat = _dil_rows(4 * n + i, S, dilation)
            for half in range(DIL_OUT // LANES):
                cols = slice(half * LANES, (half + 1) * LANES)
                dqn_ref[half, nat, :] = dq_ref[rows, cols]
                dkn_ref[half, nat, :] = dk_s[done, cols]
                dvn_ref[half, nat, :] = dv_s[done, cols]
        dk_s[CHUNK:, :] = dk_s[:DIL_W, :]
        dv_s[CHUNK:, :] = dv_s[:DIL_W, :]

    def cur(which):
        return pl.BlockSpec((None, CHUNK, DIL_OUT), lambda s: (which, nchunk - 1 - s, 0))

    def prev(which):
        return pl.BlockSpec((None, DIL_W, DIL_OUT), lambda s: (which, jnp.maximum(4 * (nchunk - 1 - s) - 1, 0), 0))

    whole = pl.BlockSpec(SPLIT(S), lambda s: (0, 0, 0))
    shp = jax.ShapeDtypeStruct(SPLIT(S), F32)
    tile = pltpu.VMEM((CHUNK, DIL_OUT), F32)
    return _pcall(body, name=name, grid=(nchunk,),
                  in_specs=[cur(0), cur(1), cur(2), prev(1), prev(2), whole, whole],
                  out_specs=[whole, whole, whole], out_shape=[shp, shp, shp],
                  scratch_shapes=[pltpu.VMEM((CHUNK + DIL_W, DIL_OUT), F32), pltpu.VMEM((CHUNK + DIL_W, DIL_OUT), F32),
                                  tile, tile, tile],
                  compiler_params=pltpu.CompilerParams(dimension_semantics=("arbitrary",),
                                                       vmem_limit_bytes=VMEM_LIMIT_RESIDENT))(
        qkv, qkv, qkv, qkv, qkv, stats, do)


def _dil_mix_fwd(os_, ls_, *, name, tm=512):
    nh, S, _ = os_[0].shape

    def body(o0, o1, o2, l0, l1, l2, out_ref):
        for half in range(nh):
            ls = [l0[half], l1[half], l2[half]]
            m = jnp.maximum(jnp.maximum(ls[0], ls[1]), ls[2])
            es = [jnp.exp(l - m) for l in ls]
            den = es[0] + es[1] + es[2]
            mixed = (es[0] * o0[half] + es[1] * o1[half] + es[2] * o2[half]) / den
            out_ref[:, half * LANES:(half + 1) * LANES] = mixed.astype(out_ref.dtype)

    halves = pl.BlockSpec((nh, tm, LANES), lambda i: (0, i, 0))
    row = pl.BlockSpec((tm, nh * LANES), lambda i: (i, 0))
    return _pcall(body, name=name, grid=(S // tm,), in_specs=[halves] * 6, out_specs=row,
                  out_shape=jax.ShapeDtypeStruct((S, nh * LANES), _CD), compiler_params=_params("parallel"))(*os_, *ls_)


def _dil_mix_bwd(doa, os_, ls_, *, name, tm=512, after=None):
    nh, S, _ = os_[0].shape

    def body(d_ref, o0, o1, o2, l0, l1, l2, do0, do1, do2, st0, st1, st2):
        first = lax.broadcasted_iota(jnp.int32, (tm, LANES), 1) % HEAD_DIM < STAT_OFFSET
        for half in range(nh):
            dv = d_ref[:, half * LANES:(half + 1) * LANES]
            ls = [l0[half], l1[half], l2[half]]
            m = jnp.maximum(jnp.maximum(ls[0], ls[1]), ls[2])
            es = [jnp.exp(l - m) for l in ls]
            den = es[0] + es[1] + es[2]
            al = [e / den for e in es]
            da = [_head_sum(dv * o[half]) for o in (o0, o1, o2)]
            mean = al[0] * da[0] + al[1] * da[1] + al[2] * da[2]
            for a, l, do_ref, st_ref in zip(al, ls, (do0, do1, do2), (st0, st1, st2)):
                do_ref[half] = a * dv
                st_ref[half] = jnp.where(first, l, -a * mean)

    halves = pl.BlockSpec((nh, tm, LANES), lambda i: (0, i, 0))
    row = pl.BlockSpec((tm, nh * LANES), lambda i: (i, 0))
    shp = jax.ShapeDtypeStruct((nh, S, LANES), F32)
    return _pcall(body, after, name=name, grid=(S // tm,), in_specs=[row] + [halves] * 6, out_specs=[halves] * 6,
                  out_shape=[shp] * 6, compiler_params=_params("parallel"))(doa, *os_, *ls_)


FOX_T = 512


PACK = 2 * HEAD_DIM
HEAD_PAIRS = N_FOX_HEADS // 2
FOX_HPS = 8
Q_BLOCK0 = 0
K_BLOCK0 = FOX_WIDTH // PACK
V_BLOCK0 = 2 * FOX_WIDTH // PACK


def _pieces(x):
    hi = x.astype(jnp.bfloat16).astype(F32)
    r = x - hi
    mid = r.astype(jnp.bfloat16).astype(F32)
    lo = (r - mid).astype(jnp.bfloat16).astype(F32)
    return [hi, mid, lo]


def _extras(first, second, rows):
    lane = lax.broadcasted_iota(jnp.int32, (rows, HEAD_DIM), 1)
    out = jnp.zeros((rows, HEAD_DIM), F32)
    for idx, val in enumerate(list(first) + list(second)):
        out = jnp.where(lane == idx, val, out)
    return out


def _head_column(c, h):
    lane = lax.broadcasted_iota(jnp.int32, c.shape, 1)
    return jnp.sum(jnp.where(lane == h, c, 0.0), axis=1, keepdims=True)


ONES3 = [1.0, 1.0, 1.0]
ZEROS3 = [0.0, 0.0, 0.0]


def _fox_pack_fwd(qkv, c, *, name, tm=512):
    S = qkv.shape[0]

    def body(q_ref, k_ref, v_ref, c_ref, qo_ref, ko_ref, vo_ref):
        hp = pl.program_id(1)
        cv = c_ref[...]
        for hh in range(2):
            ch = _pieces(_head_column(cv, 2 * hp + hh))
            src = slice(hh * HEAD_DIM, (hh + 1) * HEAD_DIM)
            lo = slice(hh * PACK, hh * PACK + HEAD_DIM)
            hi = slice(hh * PACK + HEAD_DIM, (hh + 1) * PACK)
            qo_ref[:, lo] = (q_ref[:, src].astype(F32) * ATTN_SCALE).astype(qo_ref.dtype)
            qo_ref[:, hi] = _extras(ch, ONES3, tm).astype(qo_ref.dtype)
            ko_ref[:, lo] = k_ref[:, src]
            ko_ref[:, hi] = _extras(ONES3, [-p for p in ch], tm).astype(ko_ref.dtype)
            vo_ref[:, lo] = v_ref[:, src]
            vo_ref[:, hi] = _extras(ONES3, ZEROS3, tm).astype(vo_ref.dtype)

    def src(block0):
        return pl.BlockSpec((tm, PACK), lambda i, hp: (i, block0 + hp))

    out = pl.BlockSpec((tm, 2 * PACK), lambda i, hp: (i, hp))
    shp = jax.ShapeDtypeStruct((S, N_FOX_HEADS * PACK), _CD)
    return _pcall(body, name=name, grid=(S // tm, HEAD_PAIRS),
                  in_specs=[src(Q_BLOCK0), src(K_BLOCK0), src(V_BLOCK0), pl.BlockSpec((tm, PACK), lambda i, hp: (i, 0))],
                  out_specs=[out, out, out], out_shape=[shp, shp, shp],
                  compiler_params=_params("parallel", "parallel"))(qkv, qkv, qkv, c)


def _fox_fwd(qp, kp, vp, *, name):
    S = qp.shape[0]
    nt = S // FOX_T
    nt_dims = (((1,), (1,)), ((), ()))
    tn_dims = (((0,), (0,)), ((), ()))

    def body(i_tab, j_tab, q_ref, k_ref, v_ref, o_ref, l_ref, m_s, acc_s):
        t = pl.program_id(1)
        i, j = i_tab[t], j_tab[t]

        @pl.when(j == 0)
        def _():
            m_s[...] = jnp.full((FOX_HPS, 1, FOX_T), NEG_INF, F32)
            acc_s[...] = jnp.zeros((FOX_HPS, PACK, FOX_T), F32)

        def tile(diagonal):
            for hh in range(FOX_HPS):
                cols = slice(hh * PACK, (hh + 1) * PACK)
                st = lax.dot_general(k_ref[:, cols], q_ref[:, cols], nt_dims, preferred_element_type=F32)
                if diagonal:
                    key = lax.broadcasted_iota(jnp.int32, (FOX_T, FOX_T), 0)
                    qry = lax.broadcasted_iota(jnp.int32, (FOX_T, FOX_T), 1)
                    st = jnp.where(key <= qry, st, NEG_INF)
                m_old = m_s[hh]
                m_new = jnp.maximum(m_old, jnp.max(st, axis=0, keepdims=True))
                pt = jnp.exp(st - m_new)
                acc_s[hh] = jnp.exp(m_old - m_new) * acc_s[hh] + lax.dot_general(
                    v_ref[:, cols], pt.astype(_CD), tn_dims, preferred_element_type=F32)
                m_s[hh] = m_new

        @pl.when(j < i)
        def _():
            tile(False)

        @pl.when(j == i)
        def _():
            tile(True)
            for hh in range(FOX_HPS):
                acc = acc_s[hh]
                den = acc[HEAD_DIM:HEAD_DIM + 1, :]
                cols = slice(hh * HEAD_DIM, (hh + 1) * HEAD_DIM)
                o_ref[:, cols] = (acc[:HEAD_DIM, :] / den).T
                l_ref[:, cols] = jnp.broadcast_to(m_s[hh] + jnp.log(den), (HEAD_DIM, FOX_T)).T

    pairs = [(i, j) for i in range(nt) for j in range(i + 1)]
    i_tab = jnp.asarray([p[0] for p in pairs], jnp.int32)
    j_tab = jnp.asarray([p[1] for p in pairs], jnp.int32)
    qs = pl.BlockSpec((FOX_T, FOX_HPS * PACK), lambda hp, t, it, jt: (it[t], hp))
    ks = pl.BlockSpec((FOX_T, FOX_HPS * PACK), lambda hp, t, it, jt: (jt[t], hp))
    os_ = pl.BlockSpec((FOX_T, FOX_HPS * HEAD_DIM), lambda hp, t, it, jt: (it[t], hp))
    shp = jax.ShapeDtypeStruct((S, FOX_WIDTH), F32)
    grid_spec = pltpu.PrefetchScalarGridSpec(
        num_scalar_prefetch=2, grid=(N_FOX_HEADS // FOX_HPS, len(pairs)), in_specs=[qs, ks, ks], out_specs=[os_, os_],
        scratch_shapes=[pltpu.VMEM((FOX_HPS, 1, FOX_T), F32), pltpu.VMEM((FOX_HPS, PACK, FOX_T), F32)])
    return _pcall(body, name=name, grid_spec=grid_spec, out_shape=[shp, shp],
                  compiler_params=_params("parallel", "arbitrary"))(i_tab, j_tab, qp, kp, vp)


def _fox_pack_bwd(qkv, c, o, lse, do, *, name, tm=512, after=None):
    S = qkv.shape[0]

    def body(q_ref, c_ref, o_ref, l_ref, do_ref, qo_ref, do_out_ref):
        hp = pl.program_id(1)
        cv = c_ref[...]
        for hh in range(2):
            src = slice(hh * HEAD_DIM, (hh + 1) * HEAD_DIM)
            lo = slice(hh * PACK, hh * PACK + HEAD_DIM)
            hi = slice(hh * PACK + HEAD_DIM, (hh + 1) * PACK)
            shift = _head_column(cv, 2 * hp + hh) - l_ref[:, hh * HEAD_DIM:hh * HEAD_DIM + 1]
            dov = do_ref[:, src]
            dsum = jnp.sum(dov * o_ref[:, src], axis=-1, keepdims=True)
            qo_ref[:, lo] = (q_ref[:, src].astype(F32) * ATTN_SCALE).astype(qo_ref.dtype)
            qo_ref[:, hi] = _extras(_pieces(shift), ONES3, tm).astype(qo_ref.dtype)
            do_out_ref[:, lo] = dov.astype(do_out_ref.dtype)
            do_out_ref[:, hi] = _extras(_pieces(-dsum), ZEROS3, tm).astype(do_out_ref.dtype)

    pair = pl.BlockSpec((tm, PACK), lambda i, hp: (i, hp))
    out = pl.BlockSpec((tm, 2 * PACK), lambda i, hp: (i, hp))
    shp = jax.ShapeDtypeStruct((S, N_FOX_HEADS * PACK), _CD)
    return _pcall(body, after, name=name, grid=(S // tm, HEAD_PAIRS),
                  in_specs=[pl.BlockSpec((tm, PACK), lambda i, hp: (i, Q_BLOCK0 + hp)),
                            pl.BlockSpec((tm, PACK), lambda i, hp: (i, 0)), pair, pair, pair],
                  out_specs=[out, out], out_shape=[shp, shp],
                  compiler_params=_params("parallel", "parallel"))(qkv, c, o, lse, do)


def _fox_bwd(qp, kp, vp, dop, *, name):
    S = qp.shape[0]
    nt = S // FOX_T
    nt_dims = (((1,), (1,)), ((), ()))
    tn_dims = (((0,), (0,)), ((), ()))

    def body(i_tab, j_tab, q_ref, k_ref, v_ref, do_ref, dq_ref, dk_ref, dv_ref, dc_ref, dr_ref,
             dq_s, dk_s, dv_s, dc_s, dr_s):
        t = pl.program_id(1)
        i, j = i_tab[t], j_tab[t]

        @pl.when(t == 0)
        def _():
            dq_s[...] = jnp.zeros((S, FOX_HPS * PACK), F32)
            dr_s[...] = jnp.zeros((FOX_HPS, 1, S), F32)

        @pl.when(i == j)
        def _():
            dk_s[...] = jnp.zeros((FOX_T, FOX_HPS * PACK), F32)
            dv_s[...] = jnp.zeros((FOX_T, FOX_HPS * PACK), F32)
            dc_s[...] = jnp.zeros((FOX_HPS, FOX_T, 1), F32)

        def tile(diagonal):
            rows = pl.ds(pl.multiple_of(i * FOX_T, FOX_T), FOX_T)
            for hh in range(FOX_HPS):
                cols = slice(hh * PACK, (hh + 1) * PACK)
                qv, kv, vv, dov = q_ref[:, cols], k_ref[:, cols], v_ref[:, cols], do_ref[:, cols]
                pt = jnp.exp(lax.dot_general(kv, qv, nt_dims, preferred_element_type=F32))
                if diagonal:
                    key = lax.broadcasted_iota(jnp.int32, (FOX_T, FOX_T), 0)
                    qry = lax.broadcasted_iota(jnp.int32, (FOX_T, FOX_T), 1)
                    pt = jnp.where(key <= qry, pt, 0.0)
                dst = pt * lax.dot_general(vv, dov, nt_dims, preferred_element_type=F32)
                dsb = dst.astype(_CD)
                dc_s[hh] += jnp.sum(dst, axis=1, keepdims=True)
                dr_s[hh, :, rows] += jnp.sum(dst, axis=0, keepdims=True)
                dv_s[:, cols] += jnp.dot(pt.astype(_CD), dov, preferred_element_type=F32)
                dk_s[:, cols] += jnp.dot(dsb, qv, preferred_element_type=F32)
                dq_s[rows, cols] += lax.dot_general(dsb, kv, tn_dims, preferred_element_type=F32)

        @pl.when(i > j)
        def _():
            tile(False)

        @pl.when(i == j)
        def _():
            tile(True)

        @pl.when(i == nt - 1)
        def _():
            for hh in range(FOX_HPS):
                src = slice(hh * PACK, hh * PACK + HEAD_DIM)
                dst_cols = slice(hh * HEAD_DIM, (hh + 1) * HEAD_DIM)
                dk_ref[:, dst_cols] = dk_s[:, src].astype(dk_ref.dtype)
                dv_ref[:, dst_cols] = dv_s[:, src].astype(dv_ref.dtype)
                dc_ref[:, dst_cols] = jnp.broadcast_to(dc_s[hh], (FOX_T, HEAD_DIM))

        @pl.when(t == len(pairs) - 1)
        def _():
            for hh in range(FOX_HPS):
                dq_ref[:, hh * HEAD_DIM:(hh + 1) * HEAD_DIM] = (
                    dq_s[:, hh * PACK:hh * PACK + HEAD_DIM] * ATTN_SCALE).astype(dq_ref.dtype)
            dr_ref[...] = dr_s[...]

    pairs = [(i, j) for j in range(nt) for i in range(j, nt)]
    i_tab = jnp.asarray([p[0] for p in pairs], jnp.int32)
    j_tab = jnp.asarray([p[1] for p in pairs], jnp.int32)
    wide, narrow = FOX_HPS * PACK, FOX_HPS * HEAD_DIM
    qs = pl.BlockSpec((FOX_T, wide), lambda hp, t, it, jt: (it[t], hp))
    ks = pl.BlockSpec((FOX_T, wide), lambda hp, t, it, jt: (jt[t], hp))
    whole = pl.BlockSpec((S, narrow), lambda hp, t, it, jt: (0, hp))
    cs = pl.BlockSpec((FOX_T, narrow), lambda hp, t, it, jt: (jt[t], hp))
    rs = pl.BlockSpec((FOX_HPS, 1, S), lambda hp, t, it, jt: (hp, 0, 0))
    shp = jax.ShapeDtypeStruct((S, FOX_WIDTH), _CD)
    grid_spec = pltpu.PrefetchScalarGridSpec(
        num_scalar_prefetch=2, grid=(N_FOX_HEADS // FOX_HPS, len(pairs)), in_specs=[qs, ks, ks, qs],
        out_specs=[whole, cs, cs, cs, rs],
        scratch_shapes=[pltpu.VMEM((S, wide), F32), pltpu.VMEM((FOX_T, wide), F32),
                        pltpu.VMEM((FOX_T, wide), F32), pltpu.VMEM((FOX_HPS, FOX_T, 1), F32),
                        pltpu.VMEM((FOX_HPS, 1, S), F32)])
    return _pcall(body, name=name, grid_spec=grid_spec,
                  out_shape=[shp, shp, shp, jax.ShapeDtypeStruct((S, FOX_WIDTH), F32),
                             jax.ShapeDtypeStruct((N_FOX_HEADS, 1, S), F32)],
                  compiler_params=_params("parallel", "arbitrary"))(i_tab, j_tab, qp, kp, vp, dop)


def _layer_step(x, tgt, w, p, late_weights=None, grad_sink=None, after=None, first_weights=None):
    S = x.shape[0]
    after_norm, after_proj = after if after is not None else (None, None)
    h = _rms_fwd(x, p["norm_mix_g"], name="rms_mix", after=after_norm)
    if first_weights is not None:
        w = {**w, **first_weights(h)}
    qkv = _mm(h, w["qkv"][:, 3 * DIL_WIDTH:], name="proj_fox", out_dtype=_CD, tn=768, tm=2048, after=after_proj)
    dil_qkv = _proj_dil(h, w["qkv"], name="proj_dil")
    zf = _mm(h, w["f"], name="proj_f")
    gl = _mm(h, w["g"], name="proj_gate", tn=1024, out_dtype=_CD)

    dil_o, dil_l = [], []
    for g in range(N_DIL_GROUPS):
        og, lg = _dil_fwd(dil_qkv[g], g, name=f"dil_fwd{g}")
        dil_o.append(og), dil_l.append(lg)
    o_a = _dil_mix_fwd(dil_o, dil_l, name="dil_mix")

    c = _fox_cumsum(zf, p["b_fgt"], name="fox_cumsum")
    fqp, fkp, fvp = _fox_pack_fwd(qkv, c, name="fox_pack")
    o_b, flse = _fox_fwd(fqp, fkp, fvp, name="fox_fwd")

    if late_weights is not None:
        w = {**w, **late_weights(o_b)}
    y_a = _mm(o_a, w["dil_out"], name="y_a", tn=1024, out_dtype=_CD)
    y_b = _mm(o_b, w["fox_out"], name="y_b", tn=1024, out_dtype=_CD)
    merged = _gate_fwd(gl, p["b_gate"], y_a, y_b, name="gate_fwd")
    x1 = _mm(merged, w["out"], name="mix_out", add=x)

    h2 = _rms_fwd(x1, p["norm_ffn_g"], name="rms_ffn")
    gate, up, act = _ffn_in_act(h2, w["ffn_in"], name="ffn_in")
    x2 = _mm(act, w["ffn_down"], name="ffn_down", add=x1, tk=2816)

    loss, dx2, dg_final = _loss_head(x2, p["norm_final_g"], tgt, name="loss_head")

    gw_ffn_down = _mm(act, dx2, name="gw_ffn_down", ta=True, out_dtype=_CD, tm=1408)
    dgu = _d_swiglu(dx2, w["ffn_down"], gate, up, name="d_swiglu")
    dh2 = _mm(dgu, w["ffn_in"], name="d_h2", tb=True, tk=1408, b_blocks=True, tm=2048, a_halves=True)
    gw_ffn_in = _mm(h2, dgu, name="gw_ffn_in", ta=True, out_dtype=_CD, tn=1408, out_blocks=1408, b_halves=True)
    sink = grad_sink if grad_sink is not None else (lambda group, grads: None)
    tok = sink("ffn", dict(ffn_in=gw_ffn_in, ffn_down=gw_ffn_down))
    dx1, dg_ffn = _rms_bwd(x1, p["norm_ffn_g"], dh2, dx2, name="rms_ffn_bwd", after=tok)

    dmerged = _mm(dx1, w["out"], name="d_merged", tb=True, out_dtype=_CD)
    gw_out = _mm(merged, dx1, name="gw_out", ta=True, out_dtype=_CD)
    dy_a, dy_b, dgl, db_gate = _gate_bwd(dmerged, gl, p["b_gate"], y_a, y_b, name="gate_bwd")
    do_a = _mm(dy_a, w["dil_out"], name="d_o_a", tb=True)
    gw_dil_out = _mm(o_a, dy_a, name="gw_dil_out", ta=True, out_dtype=_CD, tn=1024)
    do_b = _mm(dy_b, w["fox_out"], name="d_o_b", tb=True)
    gw_fox_out = _mm(o_b, dy_b, name="gw_fox_out", ta=True, out_dtype=_CD, tn=1024)
    tok = sink("mix", dict(dil_out=gw_dil_out, fox_out=gw_fox_out, out=gw_out))

    bqp, bdop = _fox_pack_bwd(qkv, c, o_b, flse, do_b, name="fox_pack_bwd", after=tok)
    dqp, dkp, dvp, dck, dcq = _fox_bwd(bqp, fkp, fvp, bdop, name="fox_bwd")
    dc = dcq[:, 0, :].T - dck.reshape(S, N_FOX_HEADS, HEAD_DIM)[:, :, 0]
    dc = jnp.pad(dc, ((0, 0), (0, F_PAD - N_FOX_HEADS)))
    dzf, db_fgt = _fox_cumsum_bwd(dc, zf, p["b_fgt"], name="fox_cumsum_bwd")

    douts = _dil_mix_bwd(do_a, dil_o, dil_l, name="dil_mix_bwd", after=tok)
    dqs, dks, dvs = [], [], []
    for g in range(N_DIL_GROUPS):
        dq, dk, dv = _dil_bwd(dil_qkv[g], douts[3 + g], douts[g], g, name=f"dil_bwd{g}")
        for parts, t in ((dqs, dq), (dks, dk), (dvs, dv)):
            parts.extend([t[0].astype(_CD), t[1].astype(_CD)])
    dqkv = jnp.concatenate(dqs + dks + dvs + [dqp, dkp, dvp], axis=1)

    gw_qkv = _mm(h, dqkv, name="gw_qkv", ta=True, out_dtype=_CD, tn=768)
    gw_g = _mm(h, dgl, name="gw_gate", ta=True, out_dtype=_CD)
    gw_f = _mm(h, dzf, name="gw_f", ta=True, out_dtype=_CD)
    tok = sink("in", dict(qkv=gw_qkv, f=gw_f, g=gw_g))
    dh = _mm(dqkv, w["qkv"], name="d_h_qkv", tb=True, tk=1920, tm=2048, after=tok)
    dh = _mm(dgl, w["g"], name="d_h_gate", tb=True, add=dh)
    dh = _mm(dzf, w["f"], name="d_h_f", tb=True, add=dh)
    dx, dg_mix = _rms_bwd(x, p["norm_mix_g"], dh, dx1, name="rms_mix_bwd")

    gw = dict(qkv=gw_qkv, f=gw_f, g=gw_g, dil_out=gw_dil_out, fox_out=gw_fox_out, out=gw_out, ffn_in=gw_ffn_in,
              ffn_down=gw_ffn_down)
    small = dict(norm_mix_g=dg_mix, b_fgt=db_fgt, b_gate=db_gate, norm_ffn_g=dg_ffn, norm_final_g=dg_final)
    return loss, dx, gw, small


def _position():
    return lax.axis_index("x"), lax.axis_index("y"), lax.axis_index("c")


def _other_chips(x, y):
    return [(1 - x, y), (x, 1 - y), (1 - x, 1 - y)]


ROW_TILE = 16


def _row_chunks(rows, want=4):
    n = want
    while n > 1 and rows % (n * ROW_TILE):
        n //= 2
    return n


SEM_SPEC = pl.BlockSpec(memory_space=pltpu.SEMAPHORE)
ANY_SPEC = pl.BlockSpec(memory_space=pl.ANY)
DATAFLOW = pltpu.SideEffectType.DATAFLOW_SIDE_EFFECTING


def _in_hbm(a):
    return pltpu.with_memory_space_constraint(a, pltpu.HBM)


def _split_copy_start(srcs, land_shapes, copies, after, *, name):
    n, m = len(srcs), len(land_shapes)

    def body(*refs):
        src_refs, land_refs = refs[:n], refs[n:n + m]
        send_sems, recv_sems = refs[n + m + 1], refs[n + m + 2]
        token = refs[-1]
        x, y, c = _position()
        for k, (src, dst, peer) in enumerate(copies(x, y, c, src_refs, land_refs)):
            pltpu.make_async_remote_copy(src_ref=src, dst_ref=dst, send_sem=send_sems.at[k], recv_sem=recv_sems.at[k],
                                         device_id=peer, device_id_type=MESH).start()
        token[...] = jnp.zeros_like(token)

    lands = [lax.empty(s.shape, s.dtype) for s in land_shapes]
    count = len(copies(0, 0, 0, srcs, lands))
    out = _pcall(
        body, name=name,
        out_shape=(pltpu.SemaphoreType.DMA((count,)), pltpu.SemaphoreType.DMA((count,)),
                   *[pltpu.HBM(s.shape, s.dtype) for s in srcs], *[pltpu.HBM(s.shape, s.dtype) for s in land_shapes],
                   jax.ShapeDtypeStruct((8, 128), F32)),
        in_specs=[HBM_SPEC] * (n + m) + [ANY_SPEC],
        out_specs=(SEM_SPEC, SEM_SPEC, *[HBM_SPEC] * (n + m), pl.BlockSpec(memory_space=pltpu.VMEM)),
        input_output_aliases={k: 2 + k for k in range(n + m)},
        compiler_params=pltpu.CompilerParams(has_side_effects=DATAFLOW),
    )(*[_in_hbm(s) for s in srcs], *[_in_hbm(l) for l in lands], after)
    return out[0], out[1], list(out[2:2 + n]), list(out[2 + n:2 + n + m]), out[-1]


def _split_copy_wait(send_sems, recv_sems, srcs, lands, copies, after, *, name):
    n, m = len(srcs), len(lands)

    def body(*refs):
        src_refs, land_refs = refs[:n], refs[n:n + m]
        send, recv = refs[n + m], refs[n + m + 1]
        x, y, c = _position()
        for k, (src, dst, peer) in enumerate(copies(x, y, c, src_refs, land_refs)):
            cp = pltpu.make_async_remote_copy(src_ref=src, dst_ref=dst, send_sem=send.at[k], recv_sem=recv.at[k],
                                              device_id=peer, device_id_type=MESH)
            cp.wait_send()
            cp.wait_recv()

    afters = list(after) if isinstance(after, (list, tuple)) else [after]
    out = _pcall(
        body, name=name,
        out_shape=tuple(pltpu.HBM(s.shape, s.dtype) for s in list(srcs) + list(lands)),
        in_specs=[HBM_SPEC] * (n + m) + [SEM_SPEC, SEM_SPEC] + [ANY_SPEC] * len(afters),
        out_specs=tuple([HBM_SPEC] * (n + m)),
        input_output_aliases={k: k for k in range(n + m)},
        compiler_params=pltpu.CompilerParams(has_side_effects=DATAFLOW),
    )(*srcs, *lands, send_sems, recv_sems, *afters)
    return list(out[:n]), list(out[n:])


def _gather_copies(x, y, c, shard_refs, land_refs):
    out = []
    for s, l in zip(shard_refs, land_refs):
        half = s.shape[0] // 2
        nq = _row_chunks(half)
        for cx, cy in _other_chips(x, y):
            for q in range(nq):
                rows = pl.ds(c * half + q * (half // nq), half // nq)
                out.append((s.at[rows, :], l.at[2 * x + y, rows, :], (cx, cy, c)))
    return out


def _scatter_copies(x, y, c, part_refs, land_refs):
    out = []
    for p, l in zip(part_refs, land_refs):
        nq = _row_chunks(p.shape[1])
        for r, (cx, cy) in enumerate(_other_chips(x, y)):
            for q in range(nq):
                rows = pl.ds(q * (p.shape[1] // nq), p.shape[1] // nq)
                out.append((p.at[2 * cx + cy, rows, :], l.at[r, rows, :], (cx, cy, c)))
    return out


def _forward_halves(lands, *, name):
    n = len(lands)

    def body(*refs):
        ins = refs[:n]
        send_sems, recv_sems = refs[2 * n:]
        x, y, c = _position()
        copies = []
        for w in range(n):
            half = ins[w].shape[1] // 2
            for r, (cx, cy) in enumerate(_other_chips(x, y)):
                blk = ins[w].at[2 * cx + cy, pl.ds(c * half, half), :]
                cp = pltpu.make_async_remote_copy(src_ref=blk, dst_ref=blk, send_sem=send_sems.at[w, r],
                                                  recv_sem=recv_sems.at[w, r], device_id=(x, y, 1 - c),
                                                  device_id_type=MESH)
                cp.start()
                copies.append(cp)
        for w in range(n):
            half = ins[w].shape[1] // 2
            for r, (cx, cy) in enumerate(_other_chips(x, y)):
                blk = ins[w].at[2 * cx + cy, pl.ds((1 - c) * half, half), :]
                pltpu.make_async_remote_copy(src_ref=blk, dst_ref=blk, send_sem=send_sems.at[w, r],
                                             recv_sem=recv_sems.at[w, r], device_id=(x, y, 1 - c),
                                             device_id_type=MESH).wait_recv()
        for cp in copies:
            cp.wait_send()

    return _pcall(
        body, name=name, in_specs=[HBM_SPEC] * n, out_specs=[HBM_SPEC] * n,
        out_shape=[jax.ShapeDtypeStruct(l.shape, l.dtype) for l in lands],
        input_output_aliases={k: k for k in range(n)},
        scratch_shapes=[pltpu.SemaphoreType.DMA((n, 3)), pltpu.SemaphoreType.DMA((n, 3))],
    )(*lands)


def _swap_halves(grads, name="swap_halves"):
    n = len(grads)

    def body(*refs):
        ins, outs = refs[:n], refs[n:2 * n]
        send_sems, recv_sems = refs[2 * n:]
        x, y, c = _position()
        copies = []
        for w in range(n):
            half = ins[w].shape[1] // 2
            cp = pltpu.make_async_remote_copy(
                src_ref=ins[w].at[:, pl.ds((1 - c) * half, half), :], dst_ref=outs[w], send_sem=send_sems.at[w],
                recv_sem=recv_sems.at[w], device_id=(x, y, 1 - c), device_id_type=MESH)
            cp.start()
            copies.append(cp)
        for cp in copies:
            cp.wait()

    return _pcall(
        body, name=name, in_specs=[HBM_SPEC] * n, out_specs=[HBM_SPEC] * n,
        out_shape=[jax.ShapeDtypeStruct((4, g.shape[1] // 2, g.shape[2]), g.dtype) for g in grads],
        scratch_shapes=[pltpu.SemaphoreType.DMA((n,)), pltpu.SemaphoreType.DMA((n,))],
    )(*grads)


def _share_halves(halves):
    n = len(halves)

    def body(*refs):
        ins, outs = refs[:n], refs[n:2 * n]
        send_sems, recv_sems = refs[2 * n:]
        x, y, c = _position()
        copies = []
        for w in range(n):
            cp = pltpu.make_async_remote_copy(src_ref=ins[w], dst_ref=outs[w], send_sem=send_sems.at[w],
                                              recv_sem=recv_sems.at[w], device_id=(x, y, 1 - c), device_id_type=MESH)
            cp.start()
            copies.append(cp)
        for cp in copies:
            cp.wait()

    return _pcall(
        body, name="share_halves", in_specs=[HBM_SPEC] * n, out_specs=[HBM_SPEC] * n,
        out_shape=[jax.ShapeDtypeStruct(h.shape, h.dtype) for h in halves],
        scratch_shapes=[pltpu.SemaphoreType.DMA((n,)), pltpu.SemaphoreType.DMA((n,))],
    )(*halves)


def _sum_small(part):
    rows, width = part.shape

    def body(x_ref, out_ref, all_ref, send_sems, recv_sems):
        x, y, c = _position()
        me, sibling = (x, y, c), (x, y, 1 - c)
        chips = _other_chips(x, y)

        def block(px, py, pc):
            return all_ref.at[pl.ds((4 * px + 2 * py + pc) * rows, rows), :]

        def copy(k, blk, to, src=None):
            return pltpu.make_async_remote_copy(
                src_ref=block(*blk) if src is None else src, dst_ref=block(*blk), send_sem=send_sems.at[k],
                recv_sem=recv_sems.at[k], device_id=to, device_id_type=MESH)

        all_ref[pl.ds((4 * x + 2 * y + c) * rows, rows), :] = x_ref[...]
        first = [copy(0, me, sibling, src=x_ref)]
        first += [copy(1 + j, me, (*chip, c), src=x_ref) for j, chip in enumerate(chips)]
        for cp in first:
            cp.start()
        passed = [copy(4 + j, (*chip, c), sibling) for j, chip in enumerate(chips)]
        for j, chip in enumerate(chips):
            copy(1 + j, (*chip, c), me).wait_recv()
            passed[j].start()
        copy(0, sibling, me).wait_recv()
        for j, chip in enumerate(chips):
            copy(4 + j, (*chip, 1 - c), me).wait_recv()
        for cp in first + passed:
            cp.wait_send()
        total = all_ref[0:rows, :]
        for d in range(1, 8):
            total = total + all_ref[d * rows:(d + 1) * rows, :]
        out_ref[...] = total

    vm = pl.BlockSpec(memory_space=pltpu.VMEM)
    return _pcall(
        body, name="sum_small", in_specs=[vm], out_specs=vm, out_shape=jax.ShapeDtypeStruct((rows, width), F32),
        scratch_shapes=[pltpu.VMEM((8 * rows, width), F32), pltpu.SemaphoreType.DMA((7,)), pltpu.SemaphoreType.DMA((7,))],
    )(part)


def _row_tile(R, C, itemsize=4, budget=1 << 20):
    for t in (512, 256, 128, 64, 32, 16, 8):
        if R % t == 0 and t * C * itemsize <= budget:
            return t
    return R


def _add_halves(g, recv, c, *, name):
    _, R, C = g.shape
    half = R // 2
    t = _row_tile(half, C)
    nb = half // t

    def body(c_ref, g_ref, r_ref, o_ref):
        o_ref[...] = (g_ref[...].astype(F32) + r_ref[...].astype(F32)).astype(o_ref.dtype)

    grid_spec = pltpu.PrefetchScalarGridSpec(
        num_scalar_prefetch=1, grid=(4, nb),
        in_specs=[pl.BlockSpec((1, t, C), lambda k, i, cr: (k, cr[0] * nb + i, 0)),
                  pl.BlockSpec((1, t, C), lambda k, i, cr: (k, i, 0))],
        out_specs=pl.BlockSpec((1, t, C), lambda k, i, cr: (k, i, 0)))
    return _pcall(body, name=name, grid_spec=grid_spec, out_shape=jax.ShapeDtypeStruct((4, half, C), g.dtype),
                  compiler_params=_params("parallel", "parallel"))(c, g, recv)


def _add_owners(mine, recv, *, name):
    half, C = mine.shape
    t = _row_tile(half, C)

    def body(m_ref, r_ref, o_ref):
        o_ref[...] = ((m_ref[...].astype(F32) + r_ref[0].astype(F32)) + r_ref[1].astype(F32)) + r_ref[2].astype(F32)

    return _pcall(body, name=name, grid=(half // t,),
                  in_specs=[pl.BlockSpec((t, C), lambda i: (i, 0)), pl.BlockSpec((3, t, C), lambda i: (0, i, 0))],
                  out_specs=pl.BlockSpec((t, C), lambda i: (i, 0)), out_shape=jax.ShapeDtypeStruct((half, C), F32),
                  compiler_params=_params("parallel"))(mine, recv)


def _adamw(w, g, m, v, *, name):
    R, C = w.shape
    t = _row_tile(R, C)
    c1 = 1.0 - ADAM_B1 ** ADAM_STEP
    c2 = 1.0 - ADAM_B2 ** ADAM_STEP

    def body(w_ref, g_ref, m_ref, v_ref, d_ref, nm_ref, nv_ref):
        gv = g_ref[...]
        mn = ADAM_B1 * m_ref[...] + (1.0 - ADAM_B1) * gv
        vn = ADAM_B2 * v_ref[...] + (1.0 - ADAM_B2) * (gv * gv)
        d_ref[...] = -ADAM_LR * ((mn / c1) / (jnp.sqrt(vn / c2) + ADAM_EPS) + ADAM_WD * w_ref[...])
        nm_ref[...] = mn
        nv_ref[...] = vn

    blk = pl.BlockSpec((t, C), lambda i: (i, 0))
    shp = jax.ShapeDtypeStruct((R, C), F32)
    return _pcall(body, name=name, grid=(R // t,), in_specs=[blk] * 4, out_specs=[blk] * 3, out_shape=[shp] * 3,
                  compiler_params=_params("parallel"))(w, g, m, v)


BIG = ("w_in", "w_dil_out", "w_fox_out", "w_out", "w_ffn_in", "w_ffn_down")
SMALL = ("norm_mix_g", "b_fgt", "b_gate", "norm_ffn_g", "norm_final_g")
ORDER = ("norm_mix_g", "w_in", "b_fgt", "b_gate", "w_dil_out", "w_fox_out", "w_out", "norm_ffn_g", "w_ffn_in",
         "w_ffn_down", "norm_final_g")
SMALL_ROWS = {"norm_mix_g": (0, 1), "b_gate": (1, 3), "norm_ffn_g": (3, 4), "norm_final_g": (4, 5), "b_fgt": (5, 6)}


def _columns_to_blocks(full, ncol):
    K = full.shape[0]
    return full.reshape(K, 4, ncol).transpose(1, 0, 2)


def _blocks_to_columns(blocks):
    n, K, ncol = blocks.shape
    return blocks.transpose(1, 0, 2).reshape(K, n * ncol)


def kernel(x, norm_mix_g, w_in, b_fgt, b_gate, w_dil_out, w_fox_out, w_out, norm_ffn_g, w_ffn_in, w_ffn_down, norm_final_g, loss_target, m_norm_mix_g, m_w_in, m_b_fgt, m_b_gate, m_w_dil_out, m_w_fox_out, m_w_out, m_norm_ffn_g, m_w_ffn_in, m_w_ffn_down, m_norm_final_g, v_norm_mix_g, v_w_in, v_b_fgt, v_b_gate, v_w_dil_out, v_w_fox_out, v_w_out, v_norm_ffn_g, v_w_ffn_in, v_w_ffn_down, v_norm_final_g):
    weights = dict(norm_mix_g=norm_mix_g, w_in=w_in, b_fgt=b_fgt, b_gate=b_gate, w_dil_out=w_dil_out,
                   w_fox_out=w_fox_out, w_out=w_out, norm_ffn_g=norm_ffn_g, w_ffn_in=w_ffn_in, w_ffn_down=w_ffn_down,
                   norm_final_g=norm_final_g)
    m_in = dict(norm_mix_g=m_norm_mix_g, w_in=m_w_in, b_fgt=m_b_fgt, b_gate=m_b_gate, w_dil_out=m_w_dil_out,
                w_fox_out=m_w_fox_out, w_out=m_w_out, norm_ffn_g=m_norm_ffn_g, w_ffn_in=m_w_ffn_in,
                w_ffn_down=m_w_ffn_down, norm_final_g=m_norm_final_g)
    v_in = dict(norm_mix_g=v_norm_mix_g, w_in=v_w_in, b_fgt=v_b_fgt, b_gate=v_b_gate, w_dil_out=v_w_dil_out,
                w_fox_out=v_w_fox_out, w_out=v_w_out, norm_ffn_g=v_norm_ffn_g, w_ffn_in=v_w_ffn_in,
                w_ffn_down=v_w_ffn_down, norm_final_g=v_norm_final_g)
    c = lax.axis_index("c")
    chip = 2 * lax.axis_index("x") + lax.axis_index("y")

    shards = {n: weights[n][0].astype(_CD) for n in BIG}
    in_shape = jax.ShapeDtypeStruct((4,) + shards["w_in"].shape, _CD)
    send_i, recv_i, in_src, in_land, token_in = _split_copy_start(
        [shards["w_in"]], [in_shape], _gather_copies, norm_mix_g, name="gather_in_start")
    late = BIG[1:]
    send_g, recv_g, late_src, late_land, token = _split_copy_start(
        [shards[n] for n in late], [jax.ShapeDtypeStruct((4,) + shards[n].shape, _CD) for n in late],
        _gather_copies, token_in, name="gather_late_start")
    adam_in = [t[0] + token_in[0, 0] for t in (w_in, m_w_in, v_w_in)]
    p = dict(norm_mix_g=norm_mix_g, b_fgt=jnp.pad(b_fgt, ((0, 0), (0, F_PAD - N_FOX_HEADS))), b_gate=b_gate,
             norm_ffn_g=norm_ffn_g, norm_final_g=norm_final_g.reshape(1, D_MODEL))

    def first_weights(after):
        own, lands = _split_copy_wait(send_i, recv_i, in_src, in_land, _gather_copies, [after] + adam_in,
                                      name="gather_in_wait")
        (g_in,) = _forward_halves(lands, name="gather_in_forward")
        full_in = _blocks_to_columns(lax.dynamic_update_index_in_dim(g_in, own[0], chip, 0))
        o3 = QKV_COLS
        o4 = o3 + N_FOX_HEADS
        return dict(qkv=full_in[:, :o3], f=jnp.pad(full_in[:, o3:o4], ((0, 0), (0, F_PAD - N_FOX_HEADS))),
                    g=full_in[:, o4:])

    def late_weights(after):
        own, lands = _split_copy_wait(send_g, recv_g, late_src, late_land, _gather_copies, after,
                                      name="gather_late_wait")
        lands = _forward_halves(lands, name="gather_late_forward")
        g_dil, g_fox, g_out, g_ffn_in, g_ffn_down = [
            lax.dynamic_update_index_in_dim(l, s, chip, 0) for l, s in zip(lands, own)]
        return dict(dil_out=_blocks_to_columns(g_dil), fox_out=_blocks_to_columns(g_fox),
                    out=g_out.reshape(D_MODEL, D_MODEL), ffn_in=g_ffn_in,
                    ffn_down=g_ffn_down.reshape(D_FF, D_MODEL))

    c_arr = jnp.reshape(c, (1,)).astype(jnp.int32)

    def to_blocks(n, full):
        shape = weights[n].shape
        if full.ndim == 3:
            return full
        if n in ("w_out", "w_ffn_down"):
            return full.reshape(4, shape[1], shape[2])
        return _columns_to_blocks(full, shape[2])

    def pair_sums(group, named):
        names = list(named)
        blocks = [to_blocks(n, named[n]) for n in names]
        from_sibling = _swap_halves(blocks, name=f"swap_halves_{group}")
        return [_add_halves(b, r, c_arr, name=f"add_halves_{n}") for b, r, n in zip(blocks, from_sibling, names)]

    in_flight = {}

    def grad_sink(group, gw):
        if group == "in":
            named = {"w_in": jnp.concatenate([gw["qkv"], gw["f"][:, :N_FOX_HEADS], gw["g"]], axis=1)}
        else:
            named = {"w_" + k: v for k, v in gw.items()}
        sums = pair_sums(group, named)
        started = _split_copy_start(sums, [jax.ShapeDtypeStruct((3,) + s.shape[1:], s.dtype) for s in sums],
                                    _scatter_copies, next(iter(gw.values())), name=f"scatter_{group}_start")
        in_flight[group] = (list(named), started)
        return started[-1]

    loss_part, grad_x, gw, small = _layer_step(x[0], loss_target[0], {}, p, late_weights, grad_sink,
                                               (token_in, token), first_weights)

    def owner_sums(names, sums, from_chips):
        return {n: _add_owners(lax.dynamic_index_in_dim(s, chip, 0, keepdims=False), r, name=f"add_owners_{n}")
                for n, s, r in zip(names, sums, from_chips)}

    halves = {}
    for group, (names, (send_s, recv_s, srcs, lands, _)) in in_flight.items():
        sums, from_chips = _split_copy_wait(send_s, recv_s, srcs, lands, _scatter_copies, grad_x,
                                            name=f"scatter_{group}_wait")
        halves.update(owner_sums(names, sums, from_chips))
    halves = [halves[n] for n in BIG]
    grads = {}
    for n, own, other in zip(BIG, halves, _share_halves(halves)):
        pair = jnp.stack([own, other])
        grads[n] = jnp.where(c == 0, pair, pair[::-1]).reshape(2 * own.shape[0], own.shape[1])

    packed = jnp.concatenate([
        small["norm_mix_g"], small["b_gate"].reshape(2, D_MODEL), small["norm_ffn_g"], small["norm_final_g"],
        jnp.pad(small["b_fgt"], ((0, 0), (0, D_MODEL - F_PAD))), jnp.pad(loss_part, ((0, 0), (0, D_MODEL - 1))),
        jnp.zeros((1, D_MODEL), F32)], axis=0)
    summed = _sum_small(packed)
    for n in SMALL:
        lo, hi = SMALL_ROWS[n]
        grads[n] = summed[lo:hi].reshape(1, -1)[:, :weights[n].size]
    loss = summed[6, 0]

    out_g, out_d, out_m, out_v = {}, {}, {}, {}
    for n in ORDER:
        shape = weights[n].shape
        two_d = shape[1:] if len(shape) == 3 else (1, weights[n].size)
        g2 = grads[n].reshape(two_d)
        wmv = adam_in if n == "w_in" else [t.reshape(two_d) for t in (weights[n], m_in[n], v_in[n])]
        d2, m2, v2 = _adamw(wmv[0], g2, wmv[1], wmv[2], name=f"adamw_{n}")
        out_g[n], out_d[n], out_m[n], out_v[n] = (g2.reshape(shape), d2.reshape(shape), m2.reshape(shape),
                                                  v2.reshape(shape))
    return (loss, grad_x[None], *[out_g[n] for n in ORDER], *[out_d[n] for n in ORDER],
            *[out_m[n] for n in ORDER], *[out_v[n] for n in ORDER])
```

```python
import numpy as np
import jax
import jax.numpy as jnp
from jax import lax
from jax.experimental import pallas as pl
from jax.experimental.pallas import tpu as pltpu

F32 = jnp.float32
_CD = jnp.bfloat16

D_MODEL = 1024
HEAD_DIM = 64
DIL_PAIRS = ((128, 1), (512, 4), (2048, 16))
N_DIL_GROUPS = 3
DIL_HEADS = 4
DIL_W = 128
DIL_OUT = DIL_HEADS * HEAD_DIM
DIL_WIDTH = N_DIL_GROUPS * DIL_OUT
N_FOX_HEADS = 8
FOX_WIDTH = N_FOX_HEADS * HEAD_DIM
D_FF = 2816
QKV_COLS = 3 * DIL_WIDTH + 3 * FOX_WIDTH
F_PAD = 128
RMS_EPS = 1e-6
NEG_INF = -1e30
ATTN_SCALE = HEAD_DIM ** -0.5
ADAM_LR, ADAM_B1, ADAM_B2, ADAM_EPS, ADAM_WD, ADAM_STEP = 0.001, 0.9, 0.999, 1e-08, 0.01, 10

VMEM_LIMIT = 48 * 1024 * 1024
VMEM_LIMIT_RESIDENT = 56 * 1024 * 1024
LANES = 128
MESH = pl.DeviceIdType.MESH
HBM_SPEC = pl.BlockSpec(memory_space=pltpu.HBM)


def _pcall(body, after=None, **kw):
    if after is None:
        return pl.pallas_call(body, **kw)
    n_in = len(kw["in_specs"])
    kw["in_specs"] = list(kw["in_specs"]) + [pl.BlockSpec(memory_space=pl.ANY)]

    def tied(*refs):
        return body(*refs[:n_in], *refs[n_in + 1:])

    call = pl.pallas_call(tied, **kw)
    return lambda *args: call(*args, after)


def _params(*sem):
    return pltpu.CompilerParams(dimension_semantics=sem, vmem_limit_bytes=VMEM_LIMIT)


def _pick(dim, pref):
    t = (min(pref, dim) // 128) * 128
    while t >= 128:
        if dim % t == 0:
            return t
        t -= 128
    return dim


def _mm(a, b, *, name, ta=False, tb=False, out_dtype=F32, add=None, tm=1024, tn=512, tk=2048, after=None,
        b_blocks=False, out_blocks=None, a_halves=False, b_halves=False):
    if a_halves:
        M, K = a.shape[1], 2 * a.shape[2]
    elif ta:
        K, M = a.shape
    else:
        M, K = a.shape
    if b_halves:
        b_rows, b_cols = b.shape[1], 2 * b.shape[2]
    else:
        b_rows, b_cols = (b.shape[1], b.shape[0] * b.shape[2]) if b_blocks else b.shape
    if tb:
        N, K2 = b_rows, b_cols
    else:
        K2, N = b_rows, b_cols
    assert K == K2, (a.shape, b.shape)
    shard = b.shape[2] if b_blocks else None
    tm = _pick(M, tm)
    tn = _pick(shard if (b_blocks and not tb) else (out_blocks or N), tn)
    tk = _pick(shard if (b_blocks and tb) else K, tk)
    nk = K // tk
    dn = (((0 if ta else 1,), (1 if tb else 0,)), ((), ()))
    has_add = add is not None
    assert not (has_add and out_blocks)

    def body(*refs):
        a_ref, b_ref = refs[0], refs[1]
        add_ref = refs[2] if has_add else None
        o_ref = refs[3] if has_add else refs[2]
        bv = b_ref[0] if b_blocks else b_ref[...]
        p = lax.dot_general(a_ref[...].astype(_CD), bv.astype(_CD), dn, preferred_element_type=F32)

        def finish(r):
            if has_add:
                r = r + add_ref[...]
            if out_blocks:
                o_ref[0] = r.astype(out_dtype)
            else:
                o_ref[...] = r.astype(out_dtype)

        if nk == 1:
            finish(p)
        else:
            acc_ref = refs[-1]
            k = pl.program_id(2)

            @pl.when(k == 0)
            def _():
                acc_ref[...] = p

            @pl.when(k > 0)
            def _():
                acc_ref[...] += p

            @pl.when(k == nk - 1)
            def _():
                finish(acc_ref[...])

    if a_halves:
        ka = (K // 2) // tk
        a_spec = pl.BlockSpec((None, tm, tk), lambda i, j, k: (k // ka, i, k % ka))
    else:
        a_spec = pl.BlockSpec((tk, tm), lambda i, j, k: (k, i)) if ta else pl.BlockSpec((tm, tk), lambda i, j, k: (i, k))
    if b_halves:
        nb_ = (N // 2) // tn
        b_spec = pl.BlockSpec((None, tk, tn), lambda i, j, k: (j // nb_, k, j % nb_))
    elif b_blocks and tb:
        per = shard // tk
        b_spec = pl.BlockSpec((1, tn, tk), lambda i, j, k: (k // per, j, k % per))
    elif b_blocks:
        per = shard // tn
        b_spec = pl.BlockSpec((1, tk, tn), lambda i, j, k: (j // per, k, j % per))
    else:
        b_spec = pl.BlockSpec((tn, tk), lambda i, j, k: (j, k)) if tb else pl.BlockSpec((tk, tn), lambda i, j, k: (k, j))
    if out_blocks:
        oper = out_blocks // tn
        o_spec = pl.BlockSpec((1, tm, tn), lambda i, j, k: (j // oper, i, j % oper))
        out_shape = jax.ShapeDtypeStruct((N // out_blocks, M, out_blocks), out_dtype)
    else:
        o_spec = pl.BlockSpec((tm, tn), lambda i, j, k: (i, j))
        out_shape = jax.ShapeDtypeStruct((M, N), out_dtype)
    in_specs = [a_spec, b_spec] + ([o_spec] if has_add else [])
    args = (a, b) + ((add,) if has_add else ())
    return _pcall(
        body, after, name=name, grid=(M // tm, N // tn, nk), in_specs=in_specs, out_specs=o_spec,
        out_shape=out_shape,
        scratch_shapes=[pltpu.VMEM((tm, tn), F32)] if nk > 1 else [],
        compiler_params=_params("parallel", "parallel", "arbitrary"),
    )(*args)


def _rms_fwd(x, g, *, name, tm=512, after=None):
    S, D = x.shape

    def body(x_ref, g_ref, h_ref):
        xv = x_ref[...]
        r = lax.rsqrt(jnp.mean(xv * xv, axis=-1, keepdims=True) + RMS_EPS)
        h_ref[...] = ((xv * r) * g_ref[...]).astype(h_ref.dtype)

    row = pl.BlockSpec((tm, D), lambda i: (i, 0))
    return _pcall(body, after, name=name, grid=(S // tm,), in_specs=[row, pl.BlockSpec((1, D), lambda i: (0, 0))],
                  out_specs=row, out_shape=jax.ShapeDtypeStruct((S, D), _CD), compiler_params=_params("parallel"))(x, g)


def _rms_bwd(x, g, dh, dres, *, name, tm=512, after=None):
    S, D = x.shape

    def body(x_ref, g_ref, dh_ref, dres_ref, dx_ref, dg_ref):
        xv = x_ref[...]
        r = lax.rsqrt(jnp.mean(xv * xv, axis=-1, keepdims=True) + RMS_EPS)
        xh = xv * r
        dhv = dh_ref[...]
        dxh = dhv * g_ref[...]
        dx_ref[...] = dres_ref[...] + r * (dxh - xh * jnp.mean(dxh * xh, axis=-1, keepdims=True))
        part = jnp.sum(dhv * xh, axis=0, keepdims=True)

        @pl.when(pl.program_id(0) == 0)
        def _():
            dg_ref[...] = part

        @pl.when(pl.program_id(0) > 0)
        def _():
            dg_ref[...] += part

    row = pl.BlockSpec((tm, D), lambda i: (i, 0))
    vec = pl.BlockSpec((1, D), lambda i: (0, 0))
    return _pcall(body, after, name=name, grid=(S // tm,), in_specs=[row, vec, row, row], out_specs=[row, vec],
                  out_shape=[jax.ShapeDtypeStruct((S, D), F32), jax.ShapeDtypeStruct((1, D), F32)],
                  compiler_params=_params("arbitrary"))(x, g, dh, dres)


def _loss_head(x, g, tgt, *, name, tm=512):
    S, D = x.shape

    def body(x_ref, g_ref, t_ref, loss_ref, dx_ref, dg_ref):
        xv = x_ref[...]
        gv = g_ref[...]
        r = lax.rsqrt(jnp.mean(xv * xv, axis=-1, keepdims=True) + RMS_EPS)
        xh = xv * r
        err = xh * gv - t_ref[...]
        lpart = 0.5 * jnp.sum(jnp.mean(err * err, axis=-1, keepdims=True), axis=0, keepdims=True)
        dy = err * (1.0 / D)
        dxh = dy * gv
        dx_ref[...] = r * (dxh - xh * jnp.mean(dxh * xh, axis=-1, keepdims=True))
        gpart = jnp.sum(dy * xh, axis=0, keepdims=True)

        @pl.when(pl.program_id(0) == 0)
        def _():
            loss_ref[...] = lpart
            dg_ref[...] = gpart

        @pl.when(pl.program_id(0) > 0)
        def _():
            loss_ref[...] += lpart
            dg_ref[...] += gpart

    row = pl.BlockSpec((tm, D), lambda i: (i, 0))
    vec = pl.BlockSpec((1, D), lambda i: (0, 0))
    one = pl.BlockSpec((1, 1), lambda i: (0, 0))
    return _pcall(body, name=name, grid=(S // tm,), in_specs=[row, vec, row], out_specs=[one, row, vec],
                  out_shape=[jax.ShapeDtypeStruct((1, 1), F32), jax.ShapeDtypeStruct((S, D), F32),
                             jax.ShapeDtypeStruct((1, D), F32)],
                  compiler_params=_params("arbitrary"))(x, g, tgt)


def _sigmoid(z):
    return 1.0 / (1.0 + jnp.exp(-z))


def _gate_fwd(gl, bg, ya, yb, *, name, tm=512):
    S, D = ya.shape

    def body(za_ref, zb_ref, ba_ref, bb_ref, ya_ref, yb_ref, o_ref):
        ga = _sigmoid(za_ref[...].astype(F32) + ba_ref[...])
        gb = _sigmoid(zb_ref[...].astype(F32) + bb_ref[...])
        o_ref[...] = (ga * ya_ref[...].astype(F32) + gb * yb_ref[...].astype(F32)).astype(o_ref.dtype)

    lo = pl.BlockSpec((tm, D), lambda i: (i, 0))
    hi = pl.BlockSpec((tm, D), lambda i: (i, 1))
    vlo = pl.BlockSpec((1, D), lambda i: (0, 0))
    vhi = pl.BlockSpec((1, D), lambda i: (0, 1))
    return _pcall(body, name=name, grid=(S // tm,), in_specs=[lo, hi, vlo, vhi, lo, lo], out_specs=lo,
                  out_shape=jax.ShapeDtypeStruct((S, D), _CD), compiler_params=_params("parallel"))(gl, gl, bg, bg, ya, yb)


def _gate_bwd(dm, gl, bg, ya, yb, *, name, tm=256):
    S, D = ya.shape

    def body(dm_ref, za_ref, zb_ref, ba_ref, bb_ref, ya_ref, yb_ref, dya_ref, dyb_ref, dgl_ref, dbg_ref):
        dmv = dm_ref[...].astype(F32)
        ga = _sigmoid(za_ref[...].astype(F32) + ba_ref[...])
        gb = _sigmoid(zb_ref[...].astype(F32) + bb_ref[...])
        dya_ref[...] = (dmv * ga).astype(dya_ref.dtype)
        dyb_ref[...] = (dmv * gb).astype(dyb_ref.dtype)
        dza = dmv * ya_ref[...].astype(F32) * ga * (1.0 - ga)
        dzb = dmv * yb_ref[...].astype(F32) * gb * (1.0 - gb)
        dgl_ref[:, :D] = dza.astype(dgl_ref.dtype)
        dgl_ref[:, D:] = dzb.astype(dgl_ref.dtype)
        pa = jnp.sum(dza, axis=0, keepdims=True)
        pb = jnp.sum(dzb, axis=0, keepdims=True)

        @pl.when(pl.program_id(0) == 0)
        def _():
            dbg_ref[:, :D] = pa
            dbg_ref[:, D:] = pb

        @pl.when(pl.program_id(0) > 0)
        def _():
            dbg_ref[:, :D] += pa
            dbg_ref[:, D:] += pb

    lo = pl.BlockSpec((tm, D), lambda i: (i, 0))
    hi = pl.BlockSpec((tm, D), lambda i: (i, 1))
    vlo = pl.BlockSpec((1, D), lambda i: (0, 0))
    vhi = pl.BlockSpec((1, D), lambda i: (0, 1))
    wide = pl.BlockSpec((tm, 2 * D), lambda i: (i, 0))
    vwide = pl.BlockSpec((1, 2 * D), lambda i: (0, 0))
    return _pcall(body, name=name, grid=(S // tm,), in_specs=[lo, lo, hi, vlo, vhi, lo, lo],
                  out_specs=[lo, lo, wide, vwide],
                  out_shape=[jax.ShapeDtypeStruct((S, D), _CD), jax.ShapeDtypeStruct((S, D), _CD),
                             jax.ShapeDtypeStruct((S, 2 * D), _CD), jax.ShapeDtypeStruct((1, 2 * D), F32)],
                  compiler_params=_params("arbitrary"))(dm, gl, gl, bg, bg, ya, yb)


def _ffn_in_act(h2, w_blocks, *, name, tm=512):
    S, D = h2.shape
    _, _, C = w_blocks.shape

    def body(a_ref, bg_ref, bu_ref, g_ref, u_ref, o_ref):
        av = a_ref[...].astype(_CD)
        gv = jnp.dot(av, bg_ref[0].astype(_CD), preferred_element_type=F32)
        uv = jnp.dot(av, bu_ref[0].astype(_CD), preferred_element_type=F32)
        g_ref[...] = gv.astype(g_ref.dtype)
        u_ref[...] = uv.astype(u_ref.dtype)
        o_ref[...] = (gv * _sigmoid(gv) * uv).astype(o_ref.dtype)

    out = pl.BlockSpec((tm, C), lambda i, j: (i, j))
    shp = jax.ShapeDtypeStruct((S, 2 * C), _CD)
    return _pcall(body, name=name, grid=(S // tm, 2),
                  in_specs=[pl.BlockSpec((tm, D), lambda i, j: (i, 0)), pl.BlockSpec((1, D, C), lambda i, j: (j, 0, 0)),
                            pl.BlockSpec((1, D, C), lambda i, j: (2 + j, 0, 0))],
                  out_specs=[out, out, out], out_shape=[shp, shp, shp],
                  compiler_params=_params("parallel", "arbitrary"))(h2, w_blocks, w_blocks)


def _d_swiglu(dx, w_down, gate, up, *, name, tm=512, tn=1408):
    S, D = dx.shape
    F = w_down.shape[0]
    nt = (((1,), (1,)), ((), ()))

    def body(a_ref, b_ref, g_ref, u_ref, o_ref):
        dv = lax.dot_general(a_ref[...].astype(_CD), b_ref[...].astype(_CD), nt, preferred_element_type=F32)
        gv = g_ref[...].astype(F32)
        sg = _sigmoid(gv)
        o_ref[0] = (dv * u_ref[...].astype(F32) * (sg * (1.0 + gv * (1.0 - sg)))).astype(o_ref.dtype)
        o_ref[1] = (dv * (gv * sg)).astype(o_ref.dtype)

    tile = pl.BlockSpec((tm, tn), lambda i, j: (i, j))
    return _pcall(body, name=name, grid=(S // tm, F // tn),
                  in_specs=[pl.BlockSpec((tm, D), lambda i, j: (i, 0)), pl.BlockSpec((tn, D), lambda i, j: (j, 0)),
                            tile, tile],
                  out_specs=pl.BlockSpec((2, tm, tn), lambda i, j: (0, i, j)),
                  out_shape=jax.ShapeDtypeStruct((2, S, F), _CD),
                  compiler_params=_params("parallel", "arbitrary"))(dx, w_down, gate, up)


def _split3(x):
    hi = x.astype(jnp.bfloat16)
    r1 = x - hi.astype(F32)
    mid = r1.astype(jnp.bfloat16)
    lo = (r1 - mid.astype(F32)).astype(jnp.bfloat16)
    return hi, mid, lo


def _ones_dot_left(ones, x):
    return sum(jnp.dot(ones, p, preferred_element_type=F32) for p in _split3(x))


def _ones_dot_right(x, ones):
    return sum(jnp.dot(p, ones, preferred_element_type=F32) for p in _split3(x))


def _head_sum(x):
    n = x.shape[1]
    r = lax.broadcasted_iota(jnp.int32, (n, n), 0) // HEAD_DIM
    c = lax.broadcasted_iota(jnp.int32, (n, n), 1) // HEAD_DIM
    return _ones_dot_right(x, (r == c).astype(jnp.bfloat16))


def _log_sigmoid(z):
    e = jnp.exp(-jnp.abs(z))
    t = 1.0 + e
    log1p_e = jnp.where(t == 1.0, e, jnp.log(t) * (e / jnp.where(t == 1.0, 1.0, t - 1.0)))
    return jnp.minimum(z, 0.0) - log1p_e


def _fox_cumsum(zf, bf, *, name):
    S, W = zf.shape
    nb = S // 128

    def body(z_ref, b_ref, c_ref):
        tri = (lax.broadcasted_iota(jnp.int32, (128, 128), 0) >= lax.broadcasted_iota(jnp.int32, (128, 128), 1))
        tri = tri.astype(jnp.bfloat16)

        def step(i, carry):
            rows = pl.ds(pl.multiple_of(i * 128, 128), 128)
            lf = _log_sigmoid(z_ref[rows, :] + b_ref[...])
            cb = _ones_dot_left(tri, lf) + carry
            c_ref[rows, :] = cb
            return cb[127:128, :]

        lax.fori_loop(0, nb, step, jnp.zeros((1, W), F32))

    return _pcall(body, name=name, out_shape=jax.ShapeDtypeStruct((S, W), F32),
                  compiler_params=pltpu.CompilerParams(vmem_limit_bytes=VMEM_LIMIT))(zf, bf)


def _fox_cumsum_bwd(dc, zf, bf, *, name):
    S, W = zf.shape
    nb = S // 128

    def body(dc_ref, z_ref, b_ref, dz_ref, db_ref):
        tri = (lax.broadcasted_iota(jnp.int32, (128, 128), 0) <= lax.broadcasted_iota(jnp.int32, (128, 128), 1))
        tri = tri.astype(jnp.bfloat16)

        def step(k, carry):
            tail, acc = carry
            i = nb - 1 - k
            rows = pl.ds(pl.multiple_of(i * 128, 128), 128)
            dlf = _ones_dot_left(tri, dc_ref[rows, :]) + tail
            dz = dlf * _sigmoid(-(z_ref[rows, :] + b_ref[...]))
            dz_ref[rows, :] = dz
            return dlf[0:1, :], acc + jnp.sum(dz, axis=0, keepdims=True)

        _, acc = lax.fori_loop(0, nb, step, (jnp.zeros((1, W), F32), jnp.zeros((1, W), F32)))
        db_ref[...] = acc

    return _pcall(body, name=name,
                  out_shape=[jax.ShapeDtypeStruct((S, W), F32), jax.ShapeDtypeStruct((1, W), F32)],
                  compiler_params=pltpu.CompilerParams(vmem_limit_bytes=VMEM_LIMIT))(dc, zf, bf)


def _proj_dil(h, w_qkv, *, name, tm=1024):
    S, D = h.shape
    tn = DIL_WIDTH

    def body(a_ref, b_ref, *rest):
        outs, acc = rest[:N_DIL_GROUPS], rest[N_DIL_GROUPS]
        prod = jnp.dot(a_ref[...].astype(_CD), b_ref[...].astype(_CD), preferred_element_type=F32)
        for k in range(tn // LANES):
            acc[k] = prod[:, k * LANES:(k + 1) * LANES]
        for g, (_, d) in enumerate(DIL_PAIRS):
            for half in range(DIL_OUT // LANES):
                k = g * (DIL_OUT // LANES) + half
                cols = slice(half * LANES, (half + 1) * LANES)
                for r in range(d):
                    rows = pl.ds(r, tm // d, stride=d) if d > 1 else slice(None)
                    outs[g][0, r, :, cols] = acc[k, rows, :].astype(outs[g].dtype)

    out_specs = [pl.BlockSpec((1, d, tm // d, DIL_OUT), lambda i, j: (j, 0, i, 0)) for _, d in DIL_PAIRS]
    out_shape = [jax.ShapeDtypeStruct((3, d, S // d, DIL_OUT), _CD) for _, d in DIL_PAIRS]
    outs = _pcall(body, name=name, grid=(S // tm, 3),
                  in_specs=[pl.BlockSpec((tm, D), lambda i, j: (i, 0)), pl.BlockSpec((D, tn), lambda i, j: (0, j))],
                  out_specs=out_specs, out_shape=out_shape, scratch_shapes=[pltpu.VMEM((tn // LANES, tm, LANES), F32)],
                  compiler_params=_params("parallel", "arbitrary"))(h, w_qkv)
    return [o.reshape(3, S, DIL_OUT) for o in outs]


def _dil_start(block, S, dilation):
    sub = S // dilation
    u0 = block * DIL_W
    return (u0 % sub) * dilation + u0 // sub


def _dil_slopes(group):
    h = np.arange(1, N_DIL_GROUPS * DIL_HEADS + 1, dtype=np.float32)
    s = (np.float32(2.0) ** (np.float32(-8.0) * h / np.float32(N_DIL_GROUPS * DIL_HEADS))).astype(np.float32)
    return [float(v) for v in s.reshape(N_DIL_GROUPS, DIL_HEADS)[group]]


def _dil_tiles(i, n, blocks_per_seq):
    qi = lax.broadcasted_iota(jnp.int32, (DIL_W, 2 * DIL_W), 0)
    kj = lax.broadcasted_iota(jnp.int32, (DIL_W, 2 * DIL_W), 1)
    rel = qi + DIL_W - kj
    first = ((4 * n + i) % blocks_per_seq) == 0
    valid = jnp.logical_and(jnp.logical_and(rel >= 0, rel <= DIL_W), jnp.logical_or(kj >= DIL_W, jnp.logical_not(first)))
    return valid, rel.astype(F32)


def _dil_window(cur_ref, prev_ref, i, cols):
    if i > 0:
        return cur_ref[(i - 1) * DIL_W:(i + 1) * DIL_W, cols]
    return jnp.concatenate([prev_ref[:, cols], cur_ref[:DIL_W, cols]], axis=0)


CHUNK = 4 * DIL_W


def _dil_rows(block, S, dilation):
    start = _dil_start(block, S, dilation)
    return pl.ds(start, DIL_W, stride=dilation) if dilation > 1 else pl.ds(start, DIL_W)


def SPLIT(S):
    return (DIL_OUT // LANES, S, LANES)


def _dil_fwd(qkv, group, *, name):
    S = qkv.shape[1]
    dilation = DIL_PAIRS[group][1]
    bps = (S // dilation) // DIL_W
    slopes = _dil_slopes(group)
    nt = (((1,), (1,)), ((), ()))

    def body(q_ref, k_ref, v_ref, kp_ref, vp_ref, on_ref, ln_ref, o_ref, l_ref):
        n = pl.program_id(0)
        for i in range(4):
            valid, rel = _dil_tiles(i, n, bps)
            rows = slice(i * DIL_W, (i + 1) * DIL_W)
            for h in range(DIL_HEADS):
                cols = slice(h * HEAD_DIM, (h + 1) * HEAD_DIM)
                qh = q_ref[rows, cols]
                k2, v2 = _dil_window(k_ref, kp_ref, i, cols), _dil_window(v_ref, vp_ref, i, cols)
                s = lax.dot_general(qh, k2, nt, preferred_element_type=F32) * ATTN_SCALE - (slopes[h] * dilation) * rel
                s = jnp.where(valid, s, NEG_INF)
                m = jnp.max(s, axis=-1, keepdims=True)
                p = jnp.exp(s - m)
                den = jnp.sum(p, axis=-1, keepdims=True)
                acc = jnp.dot(p.astype(_CD), v2, preferred_element_type=F32)
                o_ref[rows, cols] = acc / den
                l_ref[rows, cols] = jnp.broadcast_to(m + jnp.log(den), (DIL_W, HEAD_DIM))
        for i in range(4):
            rows = slice(i * DIL_W, (i + 1) * DIL_W)
            nat = _dil_rows(4 * n + i, S, dilation)
            for half in range(DIL_OUT // LANES):
                cols = slice(half * LANES, (half + 1) * LANES)
                on_ref[half, nat, :] = o_ref[rows, cols]
                ln_ref[half, nat, :] = l_ref[rows, cols]

    def cur(which):
        return pl.BlockSpec((None, CHUNK, DIL_OUT), lambda n: (which, n, 0))

    def prev(which):
        return pl.BlockSpec((None, DIL_W, DIL_OUT), lambda n: (which, jnp.maximum(4 * n - 1, 0), 0))

    whole = pl.BlockSpec(SPLIT(S), lambda n: (0, 0, 0))
    return _pcall(body, name=name, grid=(S // CHUNK,), in_specs=[cur(0), cur(1), cur(2), prev(1), prev(2)],
                  out_specs=[whole, whole],
                  out_shape=[jax.ShapeDtypeStruct(SPLIT(S), F32), jax.ShapeDtypeStruct(SPLIT(S), F32)],
                  scratch_shapes=[pltpu.VMEM((CHUNK, DIL_OUT), F32), pltpu.VMEM((CHUNK, DIL_OUT), F32)],
                  compiler_params=_params("arbitrary"))(qkv, qkv, qkv, qkv, qkv)


STAT_OFFSET = HEAD_DIM // 2


def _dil_bwd(qkv, stats, do, group, *, name):
    S = qkv.shape[1]
    dilation = DIL_PAIRS[group][1]
    bps = (S // dilation) // DIL_W
    slopes = _dil_slopes(group)
    nchunk = S // CHUNK
    nt = (((1,), (1,)), ((), ()))
    tn = (((0,), (0,)), ((), ()))

    def body(q_ref, k_ref, v_ref, kp_ref, vp_ref, ln_ref, don_ref, dqn_ref, dkn_ref, dvn_ref,
             dk_s, dv_s, l_ref, do_ref, dq_ref):
        step = pl.program_id(0)
        n = nchunk - 1 - step
        for i in range(4):
            rows = slice(i * DIL_W, (i + 1) * DIL_W)
            nat = _dil_rows(4 * n + i, S, dilation)
            for half in range(DIL_OUT // LANES):
                cols = slice(half * LANES, (half + 1) * LANES)
                l_ref[rows, cols] = ln_ref[half, nat, :]
                do_ref[rows, cols] = don_ref[half, nat, :]

        @pl.when(step == 0)
        def _():
            dk_s[:, CHUNK:] = jnp.zeros((DIL_OUT, DIL_W), F32)
            dv_s[:, CHUNK:] = jnp.zeros((DIL_OUT, DIL_W), F32)

        dk_s[:, :CHUNK] = jnp.zeros((DIL_OUT, CHUNK), F32)
        dv_s[:, :CHUNK] = jnp.zeros((DIL_OUT, CHUNK), F32)
        for i in range(4):
            valid, rel = _dil_tiles(i, n, bps)
            rows = slice(i * DIL_W, (i + 1) * DIL_W)
            window = slice(i * DIL_W, (i + 2) * DIL_W)
            for h in range(DIL_HEADS):
                cols = slice(h * HEAD_DIM, (h + 1) * HEAD_DIM)
                qh = q_ref[rows, cols]
                k2, v2 = _dil_window(k_ref, kp_ref, i, cols), _dil_window(v_ref, vp_ref, i, cols)
                lh = l_ref[rows, h * HEAD_DIM:h * HEAD_DIM + 1]
                shift = l_ref[rows, h * HEAD_DIM + STAT_OFFSET:h * HEAD_DIM + STAT_OFFSET + 1]
                s = lax.dot_general(qh, k2, nt, preferred_element_type=F32) * ATTN_SCALE - (slopes[h] * dilation) * rel
                p = jnp.exp(jnp.where(valid, s, NEG_INF) - lh)
                dob = do_ref[rows, cols].astype(_CD)
                ds = p * (lax.dot_general(dob, v2, nt, preferred_element_type=F32) + shift)
                dsb = (ds * ATTN_SCALE).astype(_CD)
                dq_ref[rows, cols] = jnp.dot(dsb, k2, preferred_element_type=F32)
                dk_s[cols, window] += lax.dot_general(qh, dsb, tn, preferred_element_type=F32)
                dv_s[cols, window] += lax.dot_general(dob, p.astype(_CD), tn, preferred_element_type=F32)
        for i in range(4):
            rows = slice(i * DIL_W, (i + 1) * DIL_W)
            done = slice((i + 1) * DIL_W, (i + 2) * DIL_W)
            nat = _dil_rows(4 * n + i, S, dilation)
            dkb, dvb = dk_s[:, done].T, dv_s[:, done].T
            for half in range(DIL_OUT // LANES):
                cols = slice(half * LANES, (half + 1) * LANES)
                dqn_ref[half, nat, :] = dq_ref[rows, cols]
                dkn_ref[half, nat, :] = dkb[:, cols]
                dvn_ref[half, nat, :] = dvb[:, cols]
        dk_s[:, CHUNK:] = dk_s[:, :DIL_W]
        dv_s[:, CHUNK:] = dv_s[:, :DIL_W]

    def cur(which):
        return pl.BlockSpec((None, CHUNK, DIL_OUT), lambda s: (which, nchunk - 1 - s, 0))

    def prev(which):
        return pl.BlockSpec((None, DIL_W, DIL_OUT), lambda s: (which, jnp.maximum(4 * (nchunk - 1 - s) - 1, 0), 0))

    whole = pl.BlockSpec(SPLIT(S), lambda s: (0, 0, 0))
    shp = jax.ShapeDtypeStruct(SPLIT(S), F32)
    tile = pltpu.VMEM((CHUNK, DIL_OUT), F32)
    return _pcall(body, name=name, grid=(nchunk,),
                  in_specs=[cur(0), cur(1), cur(2), prev(1), prev(2), whole, whole],
                  out_specs=[whole, whole, whole], out_shape=[shp, shp, shp],
                  scratch_shapes=[pltpu.VMEM((DIL_OUT, CHUNK + DIL_W), F32), pltpu.VMEM((DIL_OUT, CHUNK + DIL_W), F32),
                                  tile, tile, tile],
                  compiler_params=pltpu.CompilerParams(dimension_semantics=("arbitrary",),
                                                       vmem_limit_bytes=VMEM_LIMIT_RESIDENT))(
        qkv, qkv, qkv, qkv, qkv, stats, do)


def _dil_mix_fwd(os_, ls_, *, name, tm=512):
    nh, S, _ = os_[0].shape

    def body(o0, o1, o2, l0, l1, l2, out_ref):
        for half in range(nh):
            ls = [l0[half], l1[half], l2[half]]
            m = jnp.maximum(jnp.maximum(ls[0], ls[1]), ls[2])
            es = [jnp.exp(l - m) for l in ls]
            den = es[0] + es[1] + es[2]
            mixed = (es[0] * o0[half] + es[1] * o1[half] + es[2] * o2[half]) / den
            out_ref[:, half * LANES:(half + 1) * LANES] = mixed.astype(out_ref.dtype)

    halves = pl.BlockSpec((nh, tm, LANES), lambda i: (0, i, 0))
    row = pl.BlockSpec((tm, nh * LANES), lambda i: (i, 0))
    return _pcall(body, name=name, grid=(S // tm,), in_specs=[halves] * 6, out_specs=row,
                  out_shape=jax.ShapeDtypeStruct((S, nh * LANES), _CD), compiler_params=_params("parallel"))(*os_, *ls_)


def _dil_mix_bwd(doa, os_, ls_, *, name, tm=512, after=None):
    nh, S, _ = os_[0].shape

    def body(d_ref, o0, o1, o2, l0, l1, l2, do0, do1, do2, st0, st1, st2):
        first = lax.broadcasted_iota(jnp.int32, (tm, LANES), 1) % HEAD_DIM < STAT_OFFSET
        for half in range(nh):
            dv = d_ref[:, half * LANES:(half + 1) * LANES]
            ls = [l0[half], l1[half], l2[half]]
            m = jnp.maximum(jnp.maximum(ls[0], ls[1]), ls[2])
            es = [jnp.exp(l - m) for l in ls]
            den = es[0] + es[1] + es[2]
            al = [e / den for e in es]
            da = [_head_sum(dv * o[half]) for o in (o0, o1, o2)]
            mean = al[0] * da[0] + al[1] * da[1] + al[2] * da[2]
            for a, l, do_ref, st_ref in zip(al, ls, (do0, do1, do2), (st0, st1, st2)):
                do_ref[half] = a * dv
                st_ref[half] = jnp.where(first, l, -a * mean)

    halves = pl.BlockSpec((nh, tm, LANES), lambda i: (0, i, 0))
    row = pl.BlockSpec((tm, nh * LANES), lambda i: (i, 0))
    shp = jax.ShapeDtypeStruct((nh, S, LANES), F32)
    return _pcall(body, after, name=name, grid=(S // tm,), in_specs=[row] + [halves] * 6, out_specs=[halves] * 6,
                  out_shape=[shp] * 6, compiler_params=_params("parallel"))(doa, *os_, *ls_)


FOX_T = 512


PACK = 2 * HEAD_DIM
HEAD_PAIRS = N_FOX_HEADS // 2
FOX_HPS = 8
Q_BLOCK0 = 0
K_BLOCK0 = FOX_WIDTH // PACK
V_BLOCK0 = 2 * FOX_WIDTH // PACK


def _pieces(x):
    hi = x.astype(jnp.bfloat16).astype(F32)
    r = x - hi
    mid = r.astype(jnp.bfloat16).astype(F32)
    lo = (r - mid).astype(jnp.bfloat16).astype(F32)
    return [hi, mid, lo]


def _extras(first, second, rows):
    lane = lax.broadcasted_iota(jnp.int32, (rows, HEAD_DIM), 1)
    out = jnp.zeros((rows, HEAD_DIM), F32)
    for idx, val in enumerate(list(first) + list(second)):
        out = jnp.where(lane == idx, val, out)
    return out


def _head_column(c, h):
    lane = lax.broadcasted_iota(jnp.int32, c.shape, 1)
    return jnp.sum(jnp.where(lane == h, c, 0.0), axis=1, keepdims=True)


ONES3 = [1.0, 1.0, 1.0]
ZEROS3 = [0.0, 0.0, 0.0]


def _fox_pack_fwd(qkv, c, *, name, tm=512):
    S = qkv.shape[0]

    def body(q_ref, k_ref, v_ref, c_ref, qo_ref, ko_ref, vo_ref):
        hp = pl.program_id(1)
        cv = c_ref[...]
        for hh in range(2):
            ch = _pieces(_head_column(cv, 2 * hp + hh))
            src = slice(hh * HEAD_DIM, (hh + 1) * HEAD_DIM)
            lo = slice(hh * PACK, hh * PACK + HEAD_DIM)
            hi = slice(hh * PACK + HEAD_DIM, (hh + 1) * PACK)
            qo_ref[:, lo] = (q_ref[:, src].astype(F32) * ATTN_SCALE).astype(qo_ref.dtype)
            qo_ref[:, hi] = _extras(ch, ONES3, tm).astype(qo_ref.dtype)
            ko_ref[:, lo] = k_ref[:, src]
            ko_ref[:, hi] = _extras(ONES3, [-p for p in ch], tm).astype(ko_ref.dtype)
            vo_ref[:, lo] = v_ref[:, src]
            vo_ref[:, hi] = _extras(ONES3, ZEROS3, tm).astype(vo_ref.dtype)

    def src(block0):
        return pl.BlockSpec((tm, PACK), lambda i, hp: (i, block0 + hp))

    out = pl.BlockSpec((tm, 2 * PACK), lambda i, hp: (i, hp))
    shp = jax.ShapeDtypeStruct((S, N_FOX_HEADS * PACK), _CD)
    return _pcall(body, name=name, grid=(S // tm, HEAD_PAIRS),
                  in_specs=[src(Q_BLOCK0), src(K_BLOCK0), src(V_BLOCK0), pl.BlockSpec((tm, PACK), lambda i, hp: (i, 0))],
                  out_specs=[out, out, out], out_shape=[shp, shp, shp],
                  compiler_params=_params("parallel", "parallel"))(qkv, qkv, qkv, c)


def _fox_fwd(qp, kp, vp, *, name):
    S = qp.shape[0]
    nt = S // FOX_T
    nt_dims = (((1,), (1,)), ((), ()))
    tn_dims = (((0,), (0,)), ((), ()))

    def body(i_tab, j_tab, q_ref, k_ref, v_ref, o_ref, l_ref, m_s, acc_s):
        t = pl.program_id(1)
        i, j = i_tab[t], j_tab[t]

        @pl.when(j == 0)
        def _():
            m_s[...] = jnp.full((FOX_HPS, 1, FOX_T), NEG_INF, F32)
            acc_s[...] = jnp.zeros((FOX_HPS, PACK, FOX_T), F32)

        def tile(diagonal):
            for hh in range(FOX_HPS):
                cols = slice(hh * PACK, (hh + 1) * PACK)
                st = lax.dot_general(k_ref[:, cols], q_ref[:, cols], nt_dims, preferred_element_type=F32)
                if diagonal:
                    key = lax.broadcasted_iota(jnp.int32, (FOX_T, FOX_T), 0)
                    qry = lax.broadcasted_iota(jnp.int32, (FOX_T, FOX_T), 1)
                    st = jnp.where(key <= qry, st, NEG_INF)
                m_old = m_s[hh]
                m_new = jnp.maximum(m_old, jnp.max(st, axis=0, keepdims=True))
                pt = jnp.exp(st - m_new)
                acc_s[hh] = jnp.exp(m_old - m_new) * acc_s[hh] + lax.dot_general(
                    v_ref[:, cols], pt.astype(_CD), tn_dims, preferred_element_type=F32)
                m_s[hh] = m_new

        @pl.when(j < i)
        def _():
            tile(False)

        @pl.when(j == i)
        def _():
            tile(True)
            for hh in range(FOX_HPS):
                acc = acc_s[hh]
                den = acc[HEAD_DIM:HEAD_DIM + 1, :]
                cols = slice(hh * HEAD_DIM, (hh + 1) * HEAD_DIM)
                o_ref[:, cols] = (acc[:HEAD_DIM, :] / den).T
                l_ref[:, cols] = jnp.broadcast_to(m_s[hh] + jnp.log(den), (HEAD_DIM, FOX_T)).T

    pairs = [(i, j) for i in range(nt) for j in range(i + 1)]
    i_tab = jnp.asarray([p[0] for p in pairs], jnp.int32)
    j_tab = jnp.asarray([p[1] for p in pairs], jnp.int32)
    qs = pl.BlockSpec((FOX_T, FOX_HPS * PACK), lambda hp, t, it, jt: (it[t], hp))
    ks = pl.BlockSpec((FOX_T, FOX_HPS * PACK), lambda hp, t, it, jt: (jt[t], hp))
    os_ = pl.BlockSpec((FOX_T, FOX_HPS * HEAD_DIM), lambda hp, t, it, jt: (it[t], hp))
    shp = jax.ShapeDtypeStruct((S, FOX_WIDTH), F32)
    grid_spec = pltpu.PrefetchScalarGridSpec(
        num_scalar_prefetch=2, grid=(N_FOX_HEADS // FOX_HPS, len(pairs)), in_specs=[qs, ks, ks], out_specs=[os_, os_],
        scratch_shapes=[pltpu.VMEM((FOX_HPS, 1, FOX_T), F32), pltpu.VMEM((FOX_HPS, PACK, FOX_T), F32)])
    return _pcall(body, name=name, grid_spec=grid_spec, out_shape=[shp, shp],
                  compiler_params=_params("parallel", "arbitrary"))(i_tab, j_tab, qp, kp, vp)


def _fox_pack_bwd(qkv, c, o, lse, do, *, name, tm=512, after=None):
    S = qkv.shape[0]

    def body(q_ref, c_ref, o_ref, l_ref, do_ref, qo_ref, do_out_ref):
        hp = pl.program_id(1)
        cv = c_ref[...]
        for hh in range(2):
            src = slice(hh * HEAD_DIM, (hh + 1) * HEAD_DIM)
            lo = slice(hh * PACK, hh * PACK + HEAD_DIM)
            hi = slice(hh * PACK + HEAD_DIM, (hh + 1) * PACK)
            shift = _head_column(cv, 2 * hp + hh) - l_ref[:, hh * HEAD_DIM:hh * HEAD_DIM + 1]
            dov = do_ref[:, src]
            dsum = jnp.sum(dov * o_ref[:, src], axis=-1, keepdims=True)
            qo_ref[:, lo] = (q_ref[:, src].astype(F32) * ATTN_SCALE).astype(qo_ref.dtype)
            qo_ref[:, hi] = _extras(_pieces(shift), ONES3, tm).astype(qo_ref.dtype)
            do_out_ref[:, lo] = dov.astype(do_out_ref.dtype)
            do_out_ref[:, hi] = _extras(_pieces(-dsum), ZEROS3, tm).astype(do_out_ref.dtype)

    pair = pl.BlockSpec((tm, PACK), lambda i, hp: (i, hp))
    out = pl.BlockSpec((tm, 2 * PACK), lambda i, hp: (i, hp))
    shp = jax.ShapeDtypeStruct((S, N_FOX_HEADS * PACK), _CD)
    return _pcall(body, after, name=name, grid=(S // tm, HEAD_PAIRS),
                  in_specs=[pl.BlockSpec((tm, PACK), lambda i, hp: (i, Q_BLOCK0 + hp)),
                            pl.BlockSpec((tm, PACK), lambda i, hp: (i, 0)), pair, pair, pair],
                  out_specs=[out, out], out_shape=[shp, shp],
                  compiler_params=_params("parallel", "parallel"))(qkv, c, o, lse, do)


def _fox_bwd(qp, kp, vp, dop, *, name):
    S = qp.shape[0]
    nt = S // FOX_T
    nt_dims = (((1,), (1,)), ((), ()))
    tn_dims = (((0,), (0,)), ((), ()))

    def body(i_tab, j_tab, q_ref, k_ref, v_ref, do_ref, dq_ref, dk_ref, dv_ref, dc_ref, dr_ref,
             dq_s, dk_s, dv_s, dc_s, dr_s):
        t = pl.program_id(1)
        i, j = i_tab[t], j_tab[t]

        @pl.when(t == 0)
        def _():
            dq_s[...] = jnp.zeros((S, FOX_HPS * PACK), F32)
            dr_s[...] = jnp.zeros((FOX_HPS, 1, S), F32)

        @pl.when(i == j)
        def _():
            dk_s[...] = jnp.zeros((FOX_T, FOX_HPS * PACK), F32)
            dv_s[...] = jnp.zeros((FOX_T, FOX_HPS * PACK), F32)
            dc_s[...] = jnp.zeros((FOX_HPS, FOX_T, 1), F32)

        def tile(diagonal):
            rows = pl.ds(pl.multiple_of(i * FOX_T, FOX_T), FOX_T)
            for hh in range(FOX_HPS):
                cols = slice(hh * PACK, (hh + 1) * PACK)
                qv, kv, vv, dov = q_ref[:, cols], k_ref[:, cols], v_ref[:, cols], do_ref[:, cols]
                pt = jnp.exp(lax.dot_general(kv, qv, nt_dims, preferred_element_type=F32))
                if diagonal:
                    key = lax.broadcasted_iota(jnp.int32, (FOX_T, FOX_T), 0)
                    qry = lax.broadcasted_iota(jnp.int32, (FOX_T, FOX_T), 1)
                    pt = jnp.where(key <= qry, pt, 0.0)
                dst = pt * lax.dot_general(vv, dov, nt_dims, preferred_element_type=F32)
                dsb = dst.astype(_CD)
                dc_s[hh] += jnp.sum(dst, axis=1, keepdims=True)
                dr_s[hh, :, rows] += jnp.sum(dst, axis=0, keepdims=True)
                dv_s[:, cols] += jnp.dot(pt.astype(_CD), dov, preferred_element_type=F32)
                dk_s[:, cols] += jnp.dot(dsb, qv, preferred_element_type=F32)
                dq_s[rows, cols] += lax.dot_general(dsb, kv, tn_dims, preferred_element_type=F32)

        @pl.when(i > j)
        def _():
            tile(False)

        @pl.when(i == j)
        def _():
            tile(True)

        @pl.when(i == nt - 1)
        def _():
            for hh in range(FOX_HPS):
                src = slice(hh * PACK, hh * PACK + HEAD_DIM)
                dst_cols = slice(hh * HEAD_DIM, (hh + 1) * HEAD_DIM)
                dk_ref[:, dst_cols] = dk_s[:, src].astype(dk_ref.dtype)
                dv_ref[:, dst_cols] = dv_s[:, src].astype(dv_ref.dtype)
                dc_ref[:, dst_cols] = jnp.broadcast_to(dc_s[hh], (FOX_T, HEAD_DIM))

        @pl.when(t == len(pairs) - 1)
        def _():
            for hh in range(FOX_HPS):
                dq_ref[:, hh * HEAD_DIM:(hh + 1) * HEAD_DIM] = (
                    dq_s[:, hh * PACK:hh * PACK + HEAD_DIM] * ATTN_SCALE).astype(dq_ref.dtype)
            dr_ref[...] = dr_s[...]

    pairs = [(i, j) for j in range(nt) for i in range(j, nt)]
    i_tab = jnp.asarray([p[0] for p in pairs], jnp.int32)
    j_tab = jnp.asarray([p[1] for p in pairs], jnp.int32)
    wide, narrow = FOX_HPS * PACK, FOX_HPS * HEAD_DIM
    qs = pl.BlockSpec((FOX_T, wide), lambda hp, t, it, jt: (it[t], hp))
    ks = pl.BlockSpec((FOX_T, wide), lambda hp, t, it, jt: (jt[t], hp))
    whole = pl.BlockSpec((S, narrow), lambda hp, t, it, jt: (0, hp))
    cs = pl.BlockSpec((FOX_T, narrow), lambda hp, t, it, jt: (jt[t], hp))
    rs = pl.BlockSpec((FOX_HPS, 1, S), lambda hp, t, it, jt: (hp, 0, 0))
    shp = jax.ShapeDtypeStruct((S, FOX_WIDTH), _CD)
    grid_spec = pltpu.PrefetchScalarGridSpec(
        num_scalar_prefetch=2, grid=(N_FOX_HEADS // FOX_HPS, len(pairs)), in_specs=[qs, ks, ks, qs],
        out_specs=[whole, cs, cs, cs, rs],
        scratch_shapes=[pltpu.VMEM((S, wide), F32), pltpu.VMEM((FOX_T, wide), F32),
                        pltpu.VMEM((FOX_T, wide), F32), pltpu.VMEM((FOX_HPS, FOX_T, 1), F32),
                        pltpu.VMEM((FOX_HPS, 1, S), F32)])
    return _pcall(body, name=name, grid_spec=grid_spec,
                  out_shape=[shp, shp, shp, jax.ShapeDtypeStruct((S, FOX_WIDTH), F32),
                             jax.ShapeDtypeStruct((N_FOX_HEADS, 1, S), F32)],
                  compiler_params=_params("parallel", "arbitrary"))(i_tab, j_tab, qp, kp, vp, dop)


def _layer_step(x, tgt, w, p, late_weights=None, grad_sink=None, after=None, first_weights=None):
    S = x.shape[0]
    after_norm, after_proj = after if after is not None else (None, None)
    h = _rms_fwd(x, p["norm_mix_g"], name="rms_mix", after=after_norm)
    if first_weights is not None:
        w = {**w, **first_weights(h)}
    qkv = _mm(h, w["qkv"][:, 3 * DIL_WIDTH:], name="proj_fox", out_dtype=_CD, tn=768, tm=2048, after=after_proj)
    dil_qkv = _proj_dil(h, w["qkv"], name="proj_dil")
    zf = _mm(h, w["f"], name="proj_f")
    gl = _mm(h, w["g"], name="proj_gate", tn=1024, out_dtype=_CD)

    dil_o, dil_l = [], []
    for g in range(N_DIL_GROUPS):
        og, lg = _dil_fwd(dil_qkv[g], g, name=f"dil_fwd{g}")
        dil_o.append(og), dil_l.append(lg)
    o_a = _dil_mix_fwd(dil_o, dil_l, name="dil_mix")

    c = _fox_cumsum(zf, p["b_fgt"], name="fox_cumsum")
    fqp, fkp, fvp = _fox_pack_fwd(qkv, c, name="fox_pack")
    o_b, flse = _fox_fwd(fqp, fkp, fvp, name="fox_fwd")

    if late_weights is not None:
        w = {**w, **late_weights(o_b)}
    y_a = _mm(o_a, w["dil_out"], name="y_a", tn=1024, out_dtype=_CD)
    y_b = _mm(o_b, w["fox_out"], name="y_b", tn=1024, out_dtype=_CD)
    merged = _gate_fwd(gl, p["b_gate"], y_a, y_b, name="gate_fwd")
    x1 = _mm(merged, w["out"], name="mix_out", add=x)

    h2 = _rms_fwd(x1, p["norm_ffn_g"], name="rms_ffn")
    gate, up, act = _ffn_in_act(h2, w["ffn_in"], name="ffn_in")
    x2 = _mm(act, w["ffn_down"], name="ffn_down", add=x1, tk=2816)

    loss, dx2, dg_final = _loss_head(x2, p["norm_final_g"], tgt, name="loss_head")

    gw_ffn_down = _mm(act, dx2, name="gw_ffn_down", ta=True, out_dtype=_CD, tm=1408)
    dgu = _d_swiglu(dx2, w["ffn_down"], gate, up, name="d_swiglu")
    dh2 = _mm(dgu, w["ffn_in"], name="d_h2", tb=True, tk=1408, b_blocks=True, tm=2048, a_halves=True)
    gw_ffn_in = _mm(h2, dgu, name="gw_ffn_in", ta=True, out_dtype=_CD, tn=1408, out_blocks=1408, b_halves=True)
    sink = grad_sink if grad_sink is not None else (lambda group, grads: None)
    tok = sink("ffn", dict(ffn_in=gw_ffn_in, ffn_down=gw_ffn_down))
    dx1, dg_ffn = _rms_bwd(x1, p["norm_ffn_g"], dh2, dx2, name="rms_ffn_bwd", after=tok)

    dmerged = _mm(dx1, w["out"], name="d_merged", tb=True, out_dtype=_CD)
    gw_out = _mm(merged, dx1, name="gw_out", ta=True, out_dtype=_CD)
    dy_a, dy_b, dgl, db_gate = _gate_bwd(dmerged, gl, p["b_gate"], y_a, y_b, name="gate_bwd")
    do_a = _mm(dy_a, w["dil_out"], name="d_o_a", tb=True)
    gw_dil_out = _mm(o_a, dy_a, name="gw_dil_out", ta=True, out_dtype=_CD, tn=1024)
    do_b = _mm(dy_b, w["fox_out"], name="d_o_b", tb=True)
    gw_fox_out = _mm(o_b, dy_b, name="gw_fox_out", ta=True, out_dtype=_CD, tn=1024)
    tok = sink("mix", dict(dil_out=gw_dil_out, fox_out=gw_fox_out, out=gw_out))

    bqp, bdop = _fox_pack_bwd(qkv, c, o_b, flse, do_b, name="fox_pack_bwd", after=tok)
    dqp, dkp, dvp, dck, dcq = _fox_bwd(bqp, fkp, fvp, bdop, name="fox_bwd")
    dc = dcq[:, 0, :].T - dck.reshape(S, N_FOX_HEADS, HEAD_DIM)[:, :, 0]
    dc = jnp.pad(dc, ((0, 0), (0, F_PAD - N_FOX_HEADS)))
    dzf, db_fgt = _fox_cumsum_bwd(dc, zf, p["b_fgt"], name="fox_cumsum_bwd")

    douts = _dil_mix_bwd(do_a, dil_o, dil_l, name="dil_mix_bwd", after=tok)
    dqs, dks, dvs = [], [], []
    for g in range(N_DIL_GROUPS):
        dq, dk, dv = _dil_bwd(dil_qkv[g], douts[3 + g], douts[g], g, name=f"dil_bwd{g}")
        for parts, t in ((dqs, dq), (dks, dk), (dvs, dv)):
            parts.extend([t[0].astype(_CD), t[1].astype(_CD)])
    dqkv = jnp.concatenate(dqs + dks + dvs + [dqp, dkp, dvp], axis=1)

    gw_qkv = _mm(h, dqkv, name="gw_qkv", ta=True, out_dtype=_CD, tn=768)
    gw_g = _mm(h, dgl, name="gw_gate", ta=True, out_dtype=_CD)
    gw_f = _mm(h, dzf, name="gw_f", ta=True, out_dtype=_CD)
    tok = sink("in", dict(qkv=gw_qkv, f=gw_f, g=gw_g))
    dh = _mm(dqkv, w["qkv"], name="d_h_qkv", tb=True, tk=1920, tm=2048, after=tok)
    dh = _mm(dgl, w["g"], name="d_h_gate", tb=True, add=dh)
    dh = _mm(dzf, w["f"], name="d_h_f", tb=True, add=dh)
    dx, dg_mix = _rms_bwd(x, p["norm_mix_g"], dh, dx1, name="rms_mix_bwd")

    gw = dict(qkv=gw_qkv, f=gw_f, g=gw_g, dil_out=gw_dil_out, fox_out=gw_fox_out, out=gw_out, ffn_in=gw_ffn_in,
              ffn_down=gw_ffn_down)
    small = dict(norm_mix_g=dg_mix, b_fgt=db_fgt, b_gate=db_gate, norm_ffn_g=dg_ffn, norm_final_g=dg_final)
    return loss, dx, gw, small


def _position():
    return lax.axis_index("x"), lax.axis_index("y"), lax.axis_index("c")


def _other_chips(x, y):
    return [(1 - x, y), (x, 1 - y), (1 - x, 1 - y)]


ROW_TILE = 16


def _row_chunks(rows, want=4):
    n = want
    while n > 1 and rows % (n * ROW_TILE):
        n //= 2
    return n


SEM_SPEC = pl.BlockSpec(memory_space=pltpu.SEMAPHORE)
ANY_SPEC = pl.BlockSpec(memory_space=pl.ANY)
DATAFLOW = pltpu.SideEffectType.DATAFLOW_SIDE_EFFECTING


def _in_hbm(a):
    return pltpu.with_memory_space_constraint(a, pltpu.HBM)


def _split_copy_start(srcs, land_shapes, copies, after, *, name):
    n, m = len(srcs), len(land_shapes)

    def body(*refs):
        src_refs, land_refs = refs[:n], refs[n:n + m]
        send_sems, recv_sems = refs[n + m + 1], refs[n + m + 2]
        token = refs[-1]
        x, y, c = _position()
        for k, (src, dst, peer) in enumerate(copies(x, y, c, src_refs, land_refs)):
            pltpu.make_async_remote_copy(src_ref=src, dst_ref=dst, send_sem=send_sems.at[k], recv_sem=recv_sems.at[k],
                                         device_id=peer, device_id_type=MESH).start()
        token[...] = jnp.zeros_like(token)

    lands = [lax.empty(s.shape, s.dtype) for s in land_shapes]
    count = len(copies(0, 0, 0, srcs, lands))
    out = _pcall(
        body, name=name,
        out_shape=(pltpu.SemaphoreType.DMA((count,)), pltpu.SemaphoreType.DMA((count,)),
                   *[pltpu.HBM(s.shape, s.dtype) for s in srcs], *[pltpu.HBM(s.shape, s.dtype) for s in land_shapes],
                   jax.ShapeDtypeStruct((8, 128), F32)),
        in_specs=[HBM_SPEC] * (n + m) + [ANY_SPEC],
        out_specs=(SEM_SPEC, SEM_SPEC, *[HBM_SPEC] * (n + m), pl.BlockSpec(memory_space=pltpu.VMEM)),
        input_output_aliases={k: 2 + k for k in range(n + m)},
        compiler_params=pltpu.CompilerParams(has_side_effects=DATAFLOW),
    )(*[_in_hbm(s) for s in srcs], *[_in_hbm(l) for l in lands], after)
    return out[0], out[1], list(out[2:2 + n]), list(out[2 + n:2 + n + m]), out[-1]


def _split_copy_wait(send_sems, recv_sems, srcs, lands, copies, after, *, name):
    n, m = len(srcs), len(lands)

    def body(*refs):
        src_refs, land_refs = refs[:n], refs[n:n + m]
        send, recv = refs[n + m], refs[n + m + 1]
        x, y, c = _position()
        for k, (src, dst, peer) in enumerate(copies(x, y, c, src_refs, land_refs)):
            cp = pltpu.make_async_remote_copy(src_ref=src, dst_ref=dst, send_sem=send.at[k], recv_sem=recv.at[k],
                                              device_id=peer, device_id_type=MESH)
            cp.wait_send()
            cp.wait_recv()

    afters = list(after) if isinstance(after, (list, tuple)) else [after]
    out = _pcall(
        body, name=name,
        out_shape=tuple(pltpu.HBM(s.shape, s.dtype) for s in list(srcs) + list(lands)),
        in_specs=[HBM_SPEC] * (n + m) + [SEM_SPEC, SEM_SPEC] + [ANY_SPEC] * len(afters),
        out_specs=tuple([HBM_SPEC] * (n + m)),
        input_output_aliases={k: k for k in range(n + m)},
        compiler_params=pltpu.CompilerParams(has_side_effects=DATAFLOW),
    )(*srcs, *lands, send_sems, recv_sems, *afters)
    return list(out[:n]), list(out[n:])


def _gather_copies(x, y, c, shard_refs, land_refs):
    out = []
    for s, l in zip(shard_refs, land_refs):
        half = s.shape[0] // 2
        nq = _row_chunks(half)
        for cx, cy in _other_chips(x, y):
            for q in range(nq):
                rows = pl.ds(c * half + q * (half // nq), half // nq)
                out.append((s.at[rows, :], l.at[2 * x + y, rows, :], (cx, cy, c)))
    return out


def _scatter_copies(x, y, c, part_refs, land_refs):
    out = []
    for p, l in zip(part_refs, land_refs):
        nq = _row_chunks(p.shape[1])
        for r, (cx, cy) in enumerate(_other_chips(x, y)):
            for q in range(nq):
                rows = pl.ds(q * (p.shape[1] // nq), p.shape[1] // nq)
                out.append((p.at[2 * cx + cy, rows, :], l.at[r, rows, :], (cx, cy, c)))
    return out


def _forward_halves(lands, *, name):
    n = len(lands)

    def body(*refs):
        ins = refs[:n]
        send_sems, recv_sems = refs[2 * n:]
        x, y, c = _position()
        copies = []
        for w in range(n):
            half = ins[w].shape[1] // 2
            for r, (cx, cy) in enumerate(_other_chips(x, y)):
                blk = ins[w].at[2 * cx + cy, pl.ds(c * half, half), :]
                cp = pltpu.make_async_remote_copy(src_ref=blk, dst_ref=blk, send_sem=send_sems.at[w, r],
                                                  recv_sem=recv_sems.at[w, r], device_id=(x, y, 1 - c),
                                                  device_id_type=MESH)
                cp.start()
                copies.append(cp)
        for w in range(n):
            half = ins[w].shape[1] // 2
            for r, (cx, cy) in enumerate(_other_chips(x, y)):
                blk = ins[w].at[2 * cx + cy, pl.ds((1 - c) * half, half), :]
                pltpu.make_async_remote_copy(src_ref=blk, dst_ref=blk, send_sem=send_sems.at[w, r],
                                             recv_sem=recv_sems.at[w, r], device_id=(x, y, 1 - c),
                                             device_id_type=MESH).wait_recv()
        for cp in copies:
            cp.wait_send()

    return _pcall(
        body, name=name, in_specs=[HBM_SPEC] * n, out_specs=[HBM_SPEC] * n,
        out_shape=[jax.ShapeDtypeStruct(l.shape, l.dtype) for l in lands],
        input_output_aliases={k: k for k in range(n)},
        scratch_shapes=[pltpu.SemaphoreType.DMA((n, 3)), pltpu.SemaphoreType.DMA((n, 3))],
    )(*lands)


def _swap_halves(grads, name="swap_halves"):
    n = len(grads)

    def body(*refs):
        ins, outs = refs[:n], refs[n:2 * n]
        send_sems, recv_sems = refs[2 * n:]
        x, y, c = _position()
        copies = []
        for w in range(n):
            half = ins[w].shape[1] // 2
            cp = pltpu.make_async_remote_copy(
                src_ref=ins[w].at[:, pl.ds((1 - c) * half, half), :], dst_ref=outs[w], send_sem=send_sems.at[w],
                recv_sem=recv_sems.at[w], device_id=(x, y, 1 - c), device_id_type=MESH)
            cp.start()
            copies.append(cp)
        for cp in copies:
            cp.wait()

    return _pcall(
        body, name=name, in_specs=[HBM_SPEC] * n, out_specs=[HBM_SPEC] * n,
        out_shape=[jax.ShapeDtypeStruct((4, g.shape[1] // 2, g.shape[2]), g.dtype) for g in grads],
        scratch_shapes=[pltpu.SemaphoreType.DMA((n,)), pltpu.SemaphoreType.DMA((n,))],
    )(*grads)


def _share_halves(halves):
    n = len(halves)

    def body(*refs):
        ins, outs = refs[:n], refs[n:2 * n]
        send_sems, recv_sems = refs[2 * n:]
        x, y, c = _position()
        copies = []
        for w in range(n):
            cp = pltpu.make_async_remote_copy(src_ref=ins[w], dst_ref=outs[w], send_sem=send_sems.at[w],
                                              recv_sem=recv_sems.at[w], device_id=(x, y, 1 - c), device_id_type=MESH)
            cp.start()
            copies.append(cp)
        for cp in copies:
            cp.wait()

    return _pcall(
        body, name="share_halves", in_specs=[HBM_SPEC] * n, out_specs=[HBM_SPEC] * n,
        out_shape=[jax.ShapeDtypeStruct(h.shape, h.dtype) for h in halves],
        scratch_shapes=[pltpu.SemaphoreType.DMA((n,)), pltpu.SemaphoreType.DMA((n,))],
    )(*halves)


def _sum_small(part):
    rows, width = part.shape

    def body(x_ref, out_ref, all_ref, send_sems, recv_sems):
        x, y, c = _position()
        me, sibling = (x, y, c), (x, y, 1 - c)
        chips = _other_chips(x, y)

        def block(px, py, pc):
            return all_ref.at[pl.ds((4 * px + 2 * py + pc) * rows, rows), :]

        def copy(k, blk, to, src=None):
            return pltpu.make_async_remote_copy(
                src_ref=block(*blk) if src is None else src, dst_ref=block(*blk), send_sem=send_sems.at[k],
                recv_sem=recv_sems.at[k], device_id=to, device_id_type=MESH)

        all_ref[pl.ds((4 * x + 2 * y + c) * rows, rows), :] = x_ref[...]
        first = [copy(0, me, sibling, src=x_ref)]
        first += [copy(1 + j, me, (*chip, c), src=x_ref) for j, chip in enumerate(chips)]
        for cp in first:
            cp.start()
        passed = [copy(4 + j, (*chip, c), sibling) for j, chip in enumerate(chips)]
        for j, chip in enumerate(chips):
            copy(1 + j, (*chip, c), me).wait_recv()
            passed[j].start()
        copy(0, sibling, me).wait_recv()
        for j, chip in enumerate(chips):
            copy(4 + j, (*chip, 1 - c), me).wait_recv()
        for cp in first + passed:
            cp.wait_send()
        total = all_ref[0:rows, :]
        for d in range(1, 8):
            total = total + all_ref[d * rows:(d + 1) * rows, :]
        out_ref[...] = total

    vm = pl.BlockSpec(memory_space=pltpu.VMEM)
    return _pcall(
        body, name="sum_small", in_specs=[vm], out_specs=vm, out_shape=jax.ShapeDtypeStruct((rows, width), F32),
        scratch_shapes=[pltpu.VMEM((8 * rows, width), F32), pltpu.SemaphoreType.DMA((7,)), pltpu.SemaphoreType.DMA((7,))],
    )(part)


def _row_tile(R, C, itemsize=4, budget=1 << 20):
    for t in (512, 256, 128, 64, 32, 16, 8):
        if R % t == 0 and t * C * itemsize <= budget:
            return t
    return R


def _add_halves(g, recv, c, *, name):
    _, R, C = g.shape
    half = R // 2
    t = _row_tile(half, C)
    nb = half // t

    def body(c_ref, g_ref, r_ref, o_ref):
        o_ref[...] = (g_ref[...].astype(F32) + r_ref[...].astype(F32)).astype(o_ref.dtype)

    grid_spec = pltpu.PrefetchScalarGridSpec(
        num_scalar_prefetch=1, grid=(4, nb),
        in_specs=[pl.BlockSpec((1, t, C), lambda k, i, cr: (k, cr[0] * nb + i, 0)),
                  pl.BlockSpec((1, t, C), lambda k, i, cr: (k, i, 0))],
        out_specs=pl.BlockSpec((1, t, C), lambda k, i, cr: (k, i, 0)))
    return _pcall(body, name=name, grid_spec=grid_spec, out_shape=jax.ShapeDtypeStruct((4, half, C), g.dtype),
                  compiler_params=_params("parallel", "parallel"))(c, g, recv)


def _add_owners(mine, recv, *, name):
    half, C = mine.shape
    t = _row_tile(half, C)

    def body(m_ref, r_ref, o_ref):
        o_ref[...] = ((m_ref[...].astype(F32) + r_ref[0].astype(F32)) + r_ref[1].astype(F32)) + r_ref[2].astype(F32)

    return _pcall(body, name=name, grid=(half // t,),
                  in_specs=[pl.BlockSpec((t, C), lambda i: (i, 0)), pl.BlockSpec((3, t, C), lambda i: (0, i, 0))],
                  out_specs=pl.BlockSpec((t, C), lambda i: (i, 0)), out_shape=jax.ShapeDtypeStruct((half, C), F32),
                  compiler_params=_params("parallel"))(mine, recv)


def _adamw(w, g, m, v, *, name):
    R, C = w.shape
    t = _row_tile(R, C)
    c1 = 1.0 - ADAM_B1 ** ADAM_STEP
    c2 = 1.0 - ADAM_B2 ** ADAM_STEP

    def body(w_ref, g_ref, m_ref, v_ref, d_ref, nm_ref, nv_ref):
        gv = g_ref[...]
        mn = ADAM_B1 * m_ref[...] + (1.0 - ADAM_B1) * gv
        vn = ADAM_B2 * v_ref[...] + (1.0 - ADAM_B2) * (gv * gv)
        d_ref[...] = -ADAM_LR * ((mn / c1) / (jnp.sqrt(vn / c2) + ADAM_EPS) + ADAM_WD * w_ref[...])
        nm_ref[...] = mn
        nv_ref[...] = vn

    blk = pl.BlockSpec((t, C), lambda i: (i, 0))
    shp = jax.ShapeDtypeStruct((R, C), F32)
    return _pcall(body, name=name, grid=(R // t,), in_specs=[blk] * 4, out_specs=[blk] * 3, out_shape=[shp] * 3,
                  compiler_params=_params("parallel"))(w, g, m, v)


BIG = ("w_in", "w_dil_out", "w_fox_out", "w_out", "w_ffn_in", "w_ffn_down")
SMALL = ("norm_mix_g", "b_fgt", "b_gate", "norm_ffn_g", "norm_final_g")
ORDER = ("norm_mix_g", "w_in", "b_fgt", "b_gate", "w_dil_out", "w_fox_out", "w_out", "norm_ffn_g", "w_ffn_in",
         "w_ffn_down", "norm_final_g")
SMALL_ROWS = {"norm_mix_g": (0, 1), "b_gate": (1, 3), "norm_ffn_g": (3, 4), "norm_final_g": (4, 5), "b_fgt": (5, 6)}


def _columns_to_blocks(full, ncol):
    K = full.shape[0]
    return full.reshape(K, 4, ncol).transpose(1, 0, 2)


def _blocks_to_columns(blocks):
    n, K, ncol = blocks.shape
    return blocks.transpose(1, 0, 2).reshape(K, n * ncol)


def kernel(x, norm_mix_g, w_in, b_fgt, b_gate, w_dil_out, w_fox_out, w_out, norm_ffn_g, w_ffn_in, w_ffn_down, norm_final_g, loss_target, m_norm_mix_g, m_w_in, m_b_fgt, m_b_gate, m_w_dil_out, m_w_fox_out, m_w_out, m_norm_ffn_g, m_w_ffn_in, m_w_ffn_down, m_norm_final_g, v_norm_mix_g, v_w_in, v_b_fgt, v_b_gate, v_w_dil_out, v_w_fox_out, v_w_out, v_norm_ffn_g, v_w_ffn_in, v_w_ffn_down, v_norm_final_g):
    weights = dict(norm_mix_g=norm_mix_g, w_in=w_in, b_fgt=b_fgt, b_gate=b_gate, w_dil_out=w_dil_out,
                   w_fox_out=w_fox_out, w_out=w_out, norm_ffn_g=norm_ffn_g, w_ffn_in=w_ffn_in, w_ffn_down=w_ffn_down,
                   norm_final_g=norm_final_g)
    m_in = dict(norm_mix_g=m_norm_mix_g, w_in=m_w_in, b_fgt=m_b_fgt, b_gate=m_b_gate, w_dil_out=m_w_dil_out,
                w_fox_out=m_w_fox_out, w_out=m_w_out, norm_ffn_g=m_norm_ffn_g, w_ffn_in=m_w_ffn_in,
                w_ffn_down=m_w_ffn_down, norm_final_g=m_norm_final_g)
    v_in = dict(norm_mix_g=v_norm_mix_g, w_in=v_w_in, b_fgt=v_b_fgt, b_gate=v_b_gate, w_dil_out=v_w_dil_out,
                w_fox_out=v_w_fox_out, w_out=v_w_out, norm_ffn_g=v_norm_ffn_g, w_ffn_in=v_w_ffn_in,
                w_ffn_down=v_w_ffn_down, norm_final_g=v_norm_final_g)
    c = lax.axis_index("c")
    chip = 2 * lax.axis_index("x") + lax.axis_index("y")

    shards = {n: weights[n][0].astype(_CD) for n in BIG}
    in_shape = jax.ShapeDtypeStruct((4,) + shards["w_in"].shape, _CD)
    send_i, recv_i, in_src, in_land, token_in = _split_copy_start(
        [shards["w_in"]], [in_shape], _gather_copies, norm_mix_g, name="gather_in_start")
    late = BIG[1:]
    send_g, recv_g, late_src, late_land, token = _split_copy_start(
        [shards[n] for n in late], [jax.ShapeDtypeStruct((4,) + shards[n].shape, _CD) for n in late],
        _gather_copies, token_in, name="gather_late_start")
    adam_in = [t[0] + token_in[0, 0] for t in (w_in, m_w_in, v_w_in)]
    p = dict(norm_mix_g=norm_mix_g, b_fgt=jnp.pad(b_fgt, ((0, 0), (0, F_PAD - N_FOX_HEADS))), b_gate=b_gate,
             norm_ffn_g=norm_ffn_g, norm_final_g=norm_final_g.reshape(1, D_MODEL))

    def first_weights(after):
        own, lands = _split_copy_wait(send_i, recv_i, in_src, in_land, _gather_copies, [after] + adam_in,
                                      name="gather_in_wait")
        (g_in,) = _forward_halves(lands, name="gather_in_forward")
        full_in = _blocks_to_columns(lax.dynamic_update_index_in_dim(g_in, own[0], chip, 0))
        o3 = QKV_COLS
        o4 = o3 + N_FOX_HEADS
        return dict(qkv=full_in[:, :o3], f=jnp.pad(full_in[:, o3:o4], ((0, 0), (0, F_PAD - N_FOX_HEADS))),
                    g=full_in[:, o4:])

    def late_weights(after):
        own, lands = _split_copy_wait(send_g, recv_g, late_src, late_land, _gather_copies, after,
                                      name="gather_late_wait")
        lands = _forward_halves(lands, name="gather_late_forward")
        g_dil, g_fox, g_out, g_ffn_in, g_ffn_down = [
            lax.dynamic_update_index_in_dim(l, s, chip, 0) for l, s in zip(lands, own)]
        return dict(dil_out=_blocks_to_columns(g_dil), fox_out=_blocks_to_columns(g_fox),
                    out=g_out.reshape(D_MODEL, D_MODEL), ffn_in=g_ffn_in,
                    ffn_down=g_ffn_down.reshape(D_FF, D_MODEL))

    c_arr = jnp.reshape(c, (1,)).astype(jnp.int32)

    def to_blocks(n, full):
        shape = weights[n].shape
        if full.ndim == 3:
            return full
        if n in ("w_out", "w_ffn_down"):
            return full.reshape(4, shape[1], shape[2])
        return _columns_to_blocks(full, shape[2])

    def pair_sums(group, named):
        names = list(named)
        blocks = [to_blocks(n, named[n]) for n in names]
        from_sibling = _swap_halves(blocks, name=f"swap_halves_{group}")
        return [_add_halves(b, r, c_arr, name=f"add_halves_{n}") for b, r, n in zip(blocks, from_sibling, names)]

    in_flight = {}

    def grad_sink(group, gw):
        if group == "in":
            named = {"w_in": jnp.concatenate([gw["qkv"], gw["f"][:, :N_FOX_HEADS], gw["g"]], axis=1)}
        else:
            named = {"w_" + k: v for k, v in gw.items()}
        sums = pair_sums(group, named)
        started = _split_copy_start(sums, [jax.ShapeDtypeStruct((3,) + s.shape[1:], s.dtype) for s in sums],
                                    _scatter_copies, next(iter(gw.values())), name=f"scatter_{group}_start")
        in_flight[group] = (list(named), started)
        return started[-1]

    loss_part, grad_x, gw, small = _layer_step(x[0], loss_target[0], {}, p, late_weights, grad_sink,
                                               (token_in, token), first_weights)

    def owner_sums(names, sums, from_chips):
        return {n: _add_owners(lax.dynamic_index_in_dim(s, chip, 0, keepdims=False), r, name=f"add_owners_{n}")
                for n, s, r in zip(names, sums, from_chips)}

    halves = {}
    for group, (names, (send_s, recv_s, srcs, lands, _)) in in_flight.items():
        sums, from_chips = _split_copy_wait(send_s, recv_s, srcs, lands, _scatter_copies, grad_x,
                                            name=f"scatter_{group}_wait")
        halves.update(owner_sums(names, sums, from_chips))
    halves = [halves[n] for n in BIG]
    grads = {}
    for n, own, other in zip(BIG, halves, _share_halves(halves)):
        pair = jnp.stack([own, other])
        grads[n] = jnp.where(c == 0, pair, pair[::-1]).reshape(2 * own.shape[0], own.shape[1])

    packed = jnp.concatenate([
        small["norm_mix_g"], small["b_gate"].reshape(2, D_MODEL), small["norm_ffn_g"], small["norm_final_g"],
        jnp.pad(small["b_fgt"], ((0, 0), (0, D_MODEL - F_PAD))), jnp.pad(loss_part, ((0, 0), (0, D_MODEL - 1))),
        jnp.zeros((1, D_MODEL), F32)], axis=0)
    summed = _sum_small(packed)
    for n in SMALL:
        lo, hi = SMALL_ROWS[n]
        grads[n] = summed[lo:hi].reshape(1, -1)[:, :weights[n].size]
    loss = summed[6, 0]

    out_g, out_d, out_m, out_v = {}, {}, {}, {}
    for n in ORDER:
        shape = weights[n].shape
        two_d = shape[1:] if len(shape) == 3 else (1, weights[n].size)
        g2 = grads[n].reshape(two_d)
        wmv = adam_in if n == "w_in" else [t.reshape(two_d) for t in (weights[n], m_in[n], v_in[n])]
        d2, m2, v2 = _adamw(wmv[0], g2, wmv[1], wmv[2], name=f"adamw_{n}")
        out_g[n], out_d[n], out_m[n], out_v[n] = (g2.reshape(shape), d2.reshape(shape), m2.reshape(shape),
                                                  v2.reshape(shape))
    return (loss, grad_x[None], *[out_g[n] for n in ORDER], *[out_d[n] for n in ORDER],
            *[out_m[n] for n in ORDER], *[out_v[n] for n in ORDER])
```

```python
import numpy as np
import jax
import jax.numpy as jnp
from jax import lax
from jax.experimental import pallas as pl
from jax.experimental.pallas import tpu as pltpu

F32 = jnp.float32
_CD = jnp.bfloat16

D_MODEL = 1024
HEAD_DIM = 64
DIL_PAIRS = ((128, 1), (512, 4), (2048, 16))
N_DIL_GROUPS = 3
DIL_HEADS = 4
DIL_W = 128
DIL_OUT = DIL_HEADS * HEAD_DIM
DIL_WIDTH = N_DIL_GROUPS * DIL_OUT
N_FOX_HEADS = 8
FOX_WIDTH = N_FOX_HEADS * HEAD_DIM
D_FF = 2816
QKV_COLS = 3 * DIL_WIDTH + 3 * FOX_WIDTH
F_PAD = 128
RMS_EPS = 1e-6
NEG_INF = -1e30
ATTN_SCALE = HEAD_DIM ** -0.5
ADAM_LR, ADAM_B1, ADAM_B2, ADAM_EPS, ADAM_WD, ADAM_STEP = 0.001, 0.9, 0.999, 1e-08, 0.01, 10

VMEM_LIMIT = 48 * 1024 * 1024
VMEM_LIMIT_RESIDENT = 56 * 1024 * 1024
LANES = 128
MESH = pl.DeviceIdType.MESH
HBM_SPEC = pl.BlockSpec(memory_space=pltpu.HBM)


def _pcall(body, after=None, **kw):
    if after is None:
        return pl.pallas_call(body, **kw)
    n_in = len(kw["in_specs"])
    kw["in_specs"] = list(kw["in_specs"]) + [pl.BlockSpec(memory_space=pl.ANY)]

    def tied(*refs):
        return body(*refs[:n_in], *refs[n_in + 1:])

    call = pl.pallas_call(tied, **kw)
    return lambda *args: call(*args, after)


def _params(*sem):
    return pltpu.CompilerParams(dimension_semantics=sem, vmem_limit_bytes=VMEM_LIMIT)


def _pick(dim, pref):
    t = (min(pref, dim) // 128) * 128
    while t >= 128:
        if dim % t == 0:
            return t
        t -= 128
    return dim


def _mm(a, b, *, name, ta=False, tb=False, out_dtype=F32, add=None, tm=1024, tn=512, tk=2048, after=None,
        b_blocks=False, out_blocks=None, a_halves=False, b_halves=False):
    if a_halves:
        M, K = a.shape[1], 2 * a.shape[2]
    elif ta:
        K, M = a.shape
    else:
        M, K = a.shape
    if b_halves:
        b_rows, b_cols = b.shape[1], 2 * b.shape[2]
    else:
        b_rows, b_cols = (b.shape[1], b.shape[0] * b.shape[2]) if b_blocks else b.shape
    if tb:
        N, K2 = b_rows, b_cols
    else:
        K2, N = b_rows, b_cols
    assert K == K2, (a.shape, b.shape)
    shard = b.shape[2] if b_blocks else None
    tm = _pick(M, tm)
    tn = _pick(shard if (b_blocks and not tb) else (out_blocks or N), tn)
    tk = _pick(shard if (b_blocks and tb) else K, tk)
    nk = K // tk
    dn = (((0 if ta else 1,), (1 if tb else 0,)), ((), ()))
    has_add = add is not None
    assert not (has_add and out_blocks)

    def body(*refs):
        a_ref, b_ref = refs[0], refs[1]
        add_ref = refs[2] if has_add else None
        o_ref = refs[3] if has_add else refs[2]
        bv = b_ref[0] if b_blocks else b_ref[...]
        p = lax.dot_general(a_ref[...].astype(_CD), bv.astype(_CD), dn, preferred_element_type=F32)

        def finish(r):
            if has_add:
                r = r + add_ref[...]
            if out_blocks:
                o_ref[0] = r.astype(out_dtype)
            else:
                o_ref[...] = r.astype(out_dtype)

        if nk == 1:
            finish(p)
        else:
            acc_ref = refs[-1]
            k = pl.program_id(2)

            @pl.when(k == 0)
            def _():
                acc_ref[...] = p

            @pl.when(k > 0)
            def _():
                acc_ref[...] += p

            @pl.when(k == nk - 1)
            def _():
                finish(acc_ref[...])

    if a_halves:
        ka = (K // 2) // tk
        a_spec = pl.BlockSpec((None, tm, tk), lambda i, j, k: (k // ka, i, k % ka))
    else:
        a_spec = pl.BlockSpec((tk, tm), lambda i, j, k: (k, i)) if ta else pl.BlockSpec((tm, tk), lambda i, j, k: (i, k))
    if b_halves:
        nb_ = (N // 2) // tn
        b_spec = pl.BlockSpec((None, tk, tn), lambda i, j, k: (j // nb_, k, j % nb_))
    elif b_blocks and tb:
        per = shard // tk
        b_spec = pl.BlockSpec((1, tn, tk), lambda i, j, k: (k // per, j, k % per))
    elif b_blocks:
        per = shard // tn
        b_spec = pl.BlockSpec((1, tk, tn), lambda i, j, k: (j // per, k, j % per))
    else:
        b_spec = pl.BlockSpec((tn, tk), lambda i, j, k: (j, k)) if tb else pl.BlockSpec((tk, tn), lambda i, j, k: (k, j))
    if out_blocks:
        oper = out_blocks // tn
        o_spec = pl.BlockSpec((1, tm, tn), lambda i, j, k: (j // oper, i, j % oper))
        out_shape = jax.ShapeDtypeStruct((N // out_blocks, M, out_blocks), out_dtype)
    else:
        o_spec = pl.BlockSpec((tm, tn), lambda i, j, k: (i, j))
        out_shape = jax.ShapeDtypeStruct((M, N), out_dtype)
    in_specs = [a_spec, b_spec] + ([o_spec] if has_add else [])
    args = (a, b) + ((add,) if has_add else ())
    return _pcall(
        body, after, name=name, grid=(M // tm, N // tn, nk), in_specs=in_specs, out_specs=o_spec,
        out_shape=out_shape,
        scratch_shapes=[pltpu.VMEM((tm, tn), F32)] if nk > 1 else [],
        compiler_params=_params("parallel", "parallel", "arbitrary"),
    )(*args)


def _rms_fwd(x, g, *, name, tm=512, after=None):
    S, D = x.shape

    def body(x_ref, g_ref, h_ref):
        xv = x_ref[...]
        r = lax.rsqrt(jnp.mean(xv * xv, axis=-1, keepdims=True) + RMS_EPS)
        h_ref[...] = ((xv * r) * g_ref[...]).astype(h_ref.dtype)

    row = pl.BlockSpec((tm, D), lambda i: (i, 0))
    return _pcall(body, after, name=name, grid=(S // tm,), in_specs=[row, pl.BlockSpec((1, D), lambda i: (0, 0))],
                  out_specs=row, out_shape=jax.ShapeDtypeStruct((S, D), _CD), compiler_params=_params("parallel"))(x, g)


def _rms_bwd(x, g, dh, dres, *, name, tm=512, after=None):
    S, D = x.shape

    def body(x_ref, g_ref, dh_ref, dres_ref, dx_ref, dg_ref):
        xv = x_ref[...]
        r = lax.rsqrt(jnp.mean(xv * xv, axis=-1, keepdims=True) + RMS_EPS)
        xh = xv * r
        dhv = dh_ref[...]
        dxh = dhv * g_ref[...]
        dx_ref[...] = dres_ref[...] + r * (dxh - xh * jnp.mean(dxh * xh, axis=-1, keepdims=True))
        part = jnp.sum(dhv * xh, axis=0, keepdims=True)

        @pl.when(pl.program_id(0) == 0)
        def _():
            dg_ref[...] = part

        @pl.when(pl.program_id(0) > 0)
        def _():
            dg_ref[...] += part

    row = pl.BlockSpec((tm, D), lambda i: (i, 0))
    vec = pl.BlockSpec((1, D), lambda i: (0, 0))
    return _pcall(body, after, name=name, grid=(S // tm,), in_specs=[row, vec, row, row], out_specs=[row, vec],
                  out_shape=[jax.ShapeDtypeStruct((S, D), F32), jax.ShapeDtypeStruct((1, D), F32)],
                  compiler_params=_params("arbitrary"))(x, g, dh, dres)


def _loss_head(x, g, tgt, *, name, tm=512):
    S, D = x.shape

    def body(x_ref, g_ref, t_ref, loss_ref, dx_ref, dg_ref):
        xv = x_ref[...]
        gv = g_ref[...]
        r = lax.rsqrt(jnp.mean(xv * xv, axis=-1, keepdims=True) + RMS_EPS)
        xh = xv * r
        err = xh * gv - t_ref[...]
        lpart = 0.5 * jnp.sum(jnp.mean(err * err, axis=-1, keepdims=True), axis=0, keepdims=True)
        dy = err * (1.0 / D)
        dxh = dy * gv
        dx_ref[...] = r * (dxh - xh * jnp.mean(dxh * xh, axis=-1, keepdims=True))
        gpart = jnp.sum(dy * xh, axis=0, keepdims=True)

        @pl.when(pl.program_id(0) == 0)
        def _():
            loss_ref[...] = lpart
            dg_ref[...] = gpart

        @pl.when(pl.program_id(0) > 0)
        def _():
            loss_ref[...] += lpart
            dg_ref[...] += gpart

    row = pl.BlockSpec((tm, D), lambda i: (i, 0))
    vec = pl.BlockSpec((1, D), lambda i: (0, 0))
    one = pl.BlockSpec((1, 1), lambda i: (0, 0))
    return _pcall(body, name=name, grid=(S // tm,), in_specs=[row, vec, row], out_specs=[one, row, vec],
                  out_shape=[jax.ShapeDtypeStruct((1, 1), F32), jax.ShapeDtypeStruct((S, D), F32),
                             jax.ShapeDtypeStruct((1, D), F32)],
                  compiler_params=_params("arbitrary"))(x, g, tgt)


def _sigmoid(z):
    return 1.0 / (1.0 + jnp.exp(-z))


def _gate_fwd(gl, bg, ya, yb, *, name, tm=512):
    S, D = ya.shape

    def body(za_ref, zb_ref, ba_ref, bb_ref, ya_ref, yb_ref, o_ref):
        ga = _sigmoid(za_ref[...].astype(F32) + ba_ref[...])
        gb = _sigmoid(zb_ref[...].astype(F32) + bb_ref[...])
        o_ref[...] = (ga * ya_ref[...].astype(F32) + gb * yb_ref[...].astype(F32)).astype(o_ref.dtype)

    lo = pl.BlockSpec((tm, D), lambda i: (i, 0))
    hi = pl.BlockSpec((tm, D), lambda i: (i, 1))
    vlo = pl.BlockSpec((1, D), lambda i: (0, 0))
    vhi = pl.BlockSpec((1, D), lambda i: (0, 1))
    return _pcall(body, name=name, grid=(S // tm,), in_specs=[lo, hi, vlo, vhi, lo, lo], out_specs=lo,
                  out_shape=jax.ShapeDtypeStruct((S, D), _CD), compiler_params=_params("parallel"))(gl, gl, bg, bg, ya, yb)


def _gate_bwd(dm, gl, bg, ya, yb, *, name, tm=256):
    S, D = ya.shape

    def body(dm_ref, za_ref, zb_ref, ba_ref, bb_ref, ya_ref, yb_ref, dya_ref, dyb_ref, dgl_ref, dbg_ref):
        dmv = dm_ref[...].astype(F32)
        ga = _sigmoid(za_ref[...].astype(F32) + ba_ref[...])
        gb = _sigmoid(zb_ref[...].astype(F32) + bb_ref[...])
        dya_ref[...] = (dmv * ga).astype(dya_ref.dtype)
        dyb_ref[...] = (dmv * gb).astype(dyb_ref.dtype)
        dza = dmv * ya_ref[...].astype(F32) * ga * (1.0 - ga)
        dzb = dmv * yb_ref[...].astype(F32) * gb * (1.0 - gb)
        dgl_ref[:, :D] = dza.astype(dgl_ref.dtype)
        dgl_ref[:, D:] = dzb.astype(dgl_ref.dtype)
        pa = jnp.sum(dza, axis=0, keepdims=True)
        pb = jnp.sum(dzb, axis=0, keepdims=True)

        @pl.when(pl.program_id(0) == 0)
        def _():
            dbg_ref[:, :D] = pa
            dbg_ref[:, D:] = pb

        @pl.when(pl.program_id(0) > 0)
        def _():
            dbg_ref[:, :D] += pa
            dbg_ref[:, D:] += pb

    lo = pl.BlockSpec((tm, D), lambda i: (i, 0))
    hi = pl.BlockSpec((tm, D), lambda i: (i, 1))
    vlo = pl.BlockSpec((1, D), lambda i: (0, 0))
    vhi = pl.BlockSpec((1, D), lambda i: (0, 1))
    wide = pl.BlockSpec((tm, 2 * D), lambda i: (i, 0))
    vwide = pl.BlockSpec((1, 2 * D), lambda i: (0, 0))
    return _pcall(body, name=name, grid=(S // tm,), in_specs=[lo, lo, hi, vlo, vhi, lo, lo],
                  out_specs=[lo, lo, wide, vwide],
                  out_shape=[jax.ShapeDtypeStruct((S, D), _CD), jax.ShapeDtypeStruct((S, D), _CD),
                             jax.ShapeDtypeStruct((S, 2 * D), _CD), jax.ShapeDtypeStruct((1, 2 * D), F32)],
                  compiler_params=_params("arbitrary"))(dm, gl, gl, bg, bg, ya, yb)


def _ffn_in_act(h2, w_blocks, *, name, tm=512):
    S, D = h2.shape
    _, _, C = w_blocks.shape

    def body(a_ref, bg_ref, bu_ref, g_ref, u_ref, o_ref):
        av = a_ref[...].astype(_CD)
        gv = jnp.dot(av, bg_ref[0].astype(_CD), preferred_element_type=F32)
        uv = jnp.dot(av, bu_ref[0].astype(_CD), preferred_element_type=F32)
        g_ref[...] = gv.astype(g_ref.dtype)
        u_ref[...] = uv.astype(u_ref.dtype)
        o_ref[...] = (gv * _sigmoid(gv) * uv).astype(o_ref.dtype)

    out = pl.BlockSpec((tm, C), lambda i, j: (i, j))
    shp = jax.ShapeDtypeStruct((S, 2 * C), _CD)
    return _pcall(body, name=name, grid=(S // tm, 2),
                  in_specs=[pl.BlockSpec((tm, D), lambda i, j: (i, 0)), pl.BlockSpec((1, D, C), lambda i, j: (j, 0, 0)),
                            pl.BlockSpec((1, D, C), lambda i, j: (2 + j, 0, 0))],
                  out_specs=[out, out, out], out_shape=[shp, shp, shp],
                  compiler_params=_params("parallel", "arbitrary"))(h2, w_blocks, w_blocks)


def _d_swiglu(dx, w_down, gate, up, *, name, tm=512, tn=1408):
    S, D = dx.shape
    F = w_down.shape[0]
    nt = (((1,), (1,)), ((), ()))

    def body(a_ref, b_ref, g_ref, u_ref, o_ref):
        dv = lax.dot_general(a_ref[...].astype(_CD), b_ref[...].astype(_CD), nt, preferred_element_type=F32)
        gv = g_ref[...].astype(F32)
        sg = _sigmoid(gv)
        o_ref[0] = (dv * u_ref[...].astype(F32) * (sg * (1.0 + gv * (1.0 - sg)))).astype(o_ref.dtype)
        o_ref[1] = (dv * (gv * sg)).astype(o_ref.dtype)

    tile = pl.BlockSpec((tm, tn), lambda i, j: (i, j))
    return _pcall(body, name=name, grid=(S // tm, F // tn),
                  in_specs=[pl.BlockSpec((tm, D), lambda i, j: (i, 0)), pl.BlockSpec((tn, D), lambda i, j: (j, 0)),
                            tile, tile],
                  out_specs=pl.BlockSpec((2, tm, tn), lambda i, j: (0, i, j)),
                  out_shape=jax.ShapeDtypeStruct((2, S, F), _CD),
                  compiler_params=_params("parallel", "arbitrary"))(dx, w_down, gate, up)


def _split3(x):
    hi = x.astype(jnp.bfloat16)
    r1 = x - hi.astype(F32)
    mid = r1.astype(jnp.bfloat16)
    lo = (r1 - mid.astype(F32)).astype(jnp.bfloat16)
    return hi, mid, lo


def _ones_dot_left(ones, x):
    return sum(jnp.dot(ones, p, preferred_element_type=F32) for p in _split3(x))


def _ones_dot_right(x, ones):
    return sum(jnp.dot(p, ones, preferred_element_type=F32) for p in _split3(x))


def _head_sum(x):
    n = x.shape[1]
    r = lax.broadcasted_iota(jnp.int32, (n, n), 0) // HEAD_DIM
    c = lax.broadcasted_iota(jnp.int32, (n, n), 1) // HEAD_DIM
    return _ones_dot_right(x, (r == c).astype(jnp.bfloat16))


def _log_sigmoid(z):
    e = jnp.exp(-jnp.abs(z))
    t = 1.0 + e
    log1p_e = jnp.where(t == 1.0, e, jnp.log(t) * (e / jnp.where(t == 1.0, 1.0, t - 1.0)))
    return jnp.minimum(z, 0.0) - log1p_e


def _fox_cumsum(zf, bf, *, name):
    S, W = zf.shape
    nb = S // 128

    def body(z_ref, b_ref, c_ref):
        tri = (lax.broadcasted_iota(jnp.int32, (128, 128), 0) >= lax.broadcasted_iota(jnp.int32, (128, 128), 1))
        tri = tri.astype(jnp.bfloat16)

        def step(i, carry):
            rows = pl.ds(pl.multiple_of(i * 128, 128), 128)
            lf = _log_sigmoid(z_ref[rows, :] + b_ref[...])
            cb = _ones_dot_left(tri, lf) + carry
            c_ref[rows, :] = cb
            return cb[127:128, :]

        lax.fori_loop(0, nb, step, jnp.zeros((1, W), F32))

    return _pcall(body, name=name, out_shape=jax.ShapeDtypeStruct((S, W), F32),
                  compiler_params=pltpu.CompilerParams(vmem_limit_bytes=VMEM_LIMIT))(zf, bf)


def _fox_cumsum_bwd(dc, zf, bf, *, name):
    S, W = zf.shape
    nb = S // 128

    def body(dc_ref, z_ref, b_ref, dz_ref, db_ref):
        tri = (lax.broadcasted_iota(jnp.int32, (128, 128), 0) <= lax.broadcasted_iota(jnp.int32, (128, 128), 1))
        tri = tri.astype(jnp.bfloat16)

        def step(k, carry):
            tail, acc = carry
            i = nb - 1 - k
            rows = pl.ds(pl.multiple_of(i * 128, 128), 128)
            dlf = _ones_dot_left(tri, dc_ref[rows, :]) + tail
            dz = dlf * _sigmoid(-(z_ref[rows, :] + b_ref[...]))
            dz_ref[rows, :] = dz
            return dlf[0:1, :], acc + jnp.sum(dz, axis=0, keepdims=True)

        _, acc = lax.fori_loop(0, nb, step, (jnp.zeros((1, W), F32), jnp.zeros((1, W), F32)))
        db_ref[...] = acc

    return _pcall(body, name=name,
                  out_shape=[jax.ShapeDtypeStruct((S, W), F32), jax.ShapeDtypeStruct((1, W), F32)],
                  compiler_params=pltpu.CompilerParams(vmem_limit_bytes=VMEM_LIMIT))(dc, zf, bf)


def _proj_dil(h, w_qkv, *, name, tm=1024):
    S, D = h.shape
    tn = DIL_WIDTH

    def body(a_ref, b_ref, *rest):
        outs, acc = rest[:N_DIL_GROUPS], rest[N_DIL_GROUPS]
        prod = jnp.dot(a_ref[...].astype(_CD), b_ref[...].astype(_CD), preferred_element_type=F32)
        for k in range(tn // LANES):
            acc[k] = prod[:, k * LANES:(k + 1) * LANES]
        for g, (_, d) in enumerate(DIL_PAIRS):
            for half in range(DIL_OUT // LANES):
                k = g * (DIL_OUT // LANES) + half
                cols = slice(half * LANES, (half + 1) * LANES)
                for r in range(d):
                    rows = pl.ds(r, tm // d, stride=d) if d > 1 else slice(None)
                    outs[g][0, r, :, cols] = acc[k, rows, :].astype(outs[g].dtype)

    out_specs = [pl.BlockSpec((1, d, tm // d, DIL_OUT), lambda i, j: (j, 0, i, 0)) for _, d in DIL_PAIRS]
    out_shape = [jax.ShapeDtypeStruct((3, d, S // d, DIL_OUT), _CD) for _, d in DIL_PAIRS]
    outs = _pcall(body, name=name, grid=(S // tm, 3),
                  in_specs=[pl.BlockSpec((tm, D), lambda i, j: (i, 0)), pl.BlockSpec((D, tn), lambda i, j: (0, j))],
                  out_specs=out_specs, out_shape=out_shape, scratch_shapes=[pltpu.VMEM((tn // LANES, tm, LANES), F32)],
                  compiler_params=_params("parallel", "arbitrary"))(h, w_qkv)
    return [o.reshape(3, S, DIL_OUT) for o in outs]


def _dil_start(block, S, dilation):
    sub = S // dilation
    u0 = block * DIL_W
    return (u0 % sub) * dilation + u0 // sub


def _dil_slopes(group):
    h = np.arange(1, N_DIL_GROUPS * DIL_HEADS + 1, dtype=np.float32)
    s = (np.float32(2.0) ** (np.float32(-8.0) * h / np.float32(N_DIL_GROUPS * DIL_HEADS))).astype(np.float32)
    return [float(v) for v in s.reshape(N_DIL_GROUPS, DIL_HEADS)[group]]


def _dil_tiles(i, n, blocks_per_seq):
    qi = lax.broadcasted_iota(jnp.int32, (DIL_W, 2 * DIL_W), 0)
    kj = lax.broadcasted_iota(jnp.int32, (DIL_W, 2 * DIL_W), 1)
    rel = qi + DIL_W - kj
    first = ((4 * n + i) % blocks_per_seq) == 0
    valid = jnp.logical_and(jnp.logical_and(rel >= 0, rel <= DIL_W), jnp.logical_or(kj >= DIL_W, jnp.logical_not(first)))
    return valid, rel.astype(F32)


def _dil_window(cur_ref, prev_ref, i, cols):
    if i > 0:
        return cur_ref[(i - 1) * DIL_W:(i + 1) * DIL_W, cols]
    return jnp.concatenate([prev_ref[:, cols], cur_ref[:DIL_W, cols]], axis=0)


CHUNK = 4 * DIL_W


def _dil_rows(block, S, dilation):
    start = _dil_start(block, S, dilation)
    return pl.ds(start, DIL_W, stride=dilation) if dilation > 1 else pl.ds(start, DIL_W)


def SPLIT(S):
    return (DIL_OUT // LANES, S, LANES)


def _dil_fwd(qkv, group, *, name):
    S = qkv.shape[1]
    dilation = DIL_PAIRS[group][1]
    bps = (S // dilation) // DIL_W
    slopes = _dil_slopes(group)
    nt = (((1,), (1,)), ((), ()))

    def body(q_ref, k_ref, v_ref, kp_ref, vp_ref, on_ref, ln_ref, o_ref, l_ref):
        n = pl.program_id(0)
        for i in range(4):
            valid, rel = _dil_tiles(i, n, bps)
            rows = slice(i * DIL_W, (i + 1) * DIL_W)
            for h in range(DIL_HEADS):
                cols = slice(h * HEAD_DIM, (h + 1) * HEAD_DIM)
                qh = q_ref[rows, cols]
                k2, v2 = _dil_window(k_ref, kp_ref, i, cols), _dil_window(v_ref, vp_ref, i, cols)
                s = lax.dot_general(qh, k2, nt, preferred_element_type=F32) * ATTN_SCALE - (slopes[h] * dilation) * rel
                s = jnp.where(valid, s, NEG_INF)
                m = jnp.max(s, axis=-1, keepdims=True)
                p = jnp.exp(s - m)
                den = jnp.sum(p, axis=-1, keepdims=True)
                acc = jnp.dot(p.astype(_CD), v2, preferred_element_type=F32)
                o_ref[rows, cols] = acc / den
                l_ref[rows, cols] = jnp.broadcast_to(m + jnp.log(den), (DIL_W, HEAD_DIM))
        for i in range(4):
            rows = slice(i * DIL_W, (i + 1) * DIL_W)
            nat = _dil_rows(4 * n + i, S, dilation)
            for half in range(DIL_OUT // LANES):
                cols = slice(half * LANES, (half + 1) * LANES)
                on_ref[half, nat, :] = o_ref[rows, cols]
                ln_ref[half, nat, :] = l_ref[rows, cols]

    def cur(which):
        return pl.BlockSpec((None, CHUNK, DIL_OUT), lambda n: (which, n, 0))

    def prev(which):
        return pl.BlockSpec((None, DIL_W, DIL_OUT), lambda n: (which, jnp.maximum(4 * n - 1, 0), 0))

    whole = pl.BlockSpec(SPLIT(S), lambda n: (0, 0, 0))
    return _pcall(body, name=name, grid=(S // CHUNK,), in_specs=[cur(0), cur(1), cur(2), prev(1), prev(2)],
                  out_specs=[whole, whole],
                  out_shape=[jax.ShapeDtypeStruct(SPLIT(S), F32), jax.ShapeDtypeStruct(SPLIT(S), F32)],
                  scratch_shapes=[pltpu.VMEM((CHUNK, DIL_OUT), F32), pltpu.VMEM((CHUNK, DIL_OUT), F32)],
                  compiler_params=_params("arbitrary"))(qkv, qkv, qkv, qkv, qkv)


STAT_OFFSET = HEAD_DIM // 2


def _dil_bwd(qkv, stats, do, group, *, name):
    S = qkv.shape[1]
    dilation = DIL_PAIRS[group][1]
    bps = (S // dilation) // DIL_W
    slopes = _dil_slopes(group)
    nchunk = S // CHUNK
    nt = (((1,), (1,)), ((), ()))
    tn = (((0,), (0,)), ((), ()))

    def body(q_ref, k_ref, v_ref, kp_ref, vp_ref, ln_ref, don_ref, dqn_ref, dkn_ref, dvn_ref,
             dk_s, dv_s, l_ref, do_ref, dq_ref):
        step = pl.program_id(0)
        n = nchunk - 1 - step
        for i in range(4):
            rows = slice(i * DIL_W, (i + 1) * DIL_W)
            nat = _dil_rows(4 * n + i, S, dilation)
            for half in range(DIL_OUT // LANES):
                cols = slice(half * LANES, (half + 1) * LANES)
                l_ref[rows, cols] = ln_ref[half, nat, :]
                do_ref[rows, cols] = don_ref[half, nat, :]

        @pl.when(step == 0)
        def _():
            dk_s[:, CHUNK:] = jnp.zeros((DIL_OUT, DIL_W), F32)
            dv_s[:, CHUNK:] = jnp.zeros((DIL_OUT, DIL_W), F32)

        dk_s[:, :CHUNK] = jnp.zeros((DIL_OUT, CHUNK), F32)
        dv_s[:, :CHUNK] = jnp.zeros((DIL_OUT, CHUNK), F32)
        for i in range(4):
            valid, rel = _dil_tiles(i, n, bps)
            rows = slice(i * DIL_W, (i + 1) * DIL_W)
            window = slice(i * DIL_W, (i + 2) * DIL_W)
            for h in range(DIL_HEADS):
                cols = slice(h * HEAD_DIM, (h + 1) * HEAD_DIM)
                qh = q_ref[rows, cols]
                k2, v2 = _dil_window(k_ref, kp_ref, i, cols), _dil_window(v_ref, vp_ref, i, cols)
                lh = l_ref[rows, h * HEAD_DIM:h * HEAD_DIM + 1]
                shift = l_ref[rows, h * HEAD_DIM + STAT_OFFSET:h * HEAD_DIM + STAT_OFFSET + 1]
                s = lax.dot_general(qh, k2, nt, preferred_element_type=F32) * ATTN_SCALE - (slopes[h] * dilation) * rel
                p = jnp.exp(jnp.where(valid, s, NEG_INF) - lh)
                dob = do_ref[rows, cols].astype(_CD)
                ds = p * (lax.dot_general(dob, v2, nt, preferred_element_type=F32) + shift)
                dsb = (ds * ATTN_SCALE).astype(_CD)
                dq_ref[rows, cols] = jnp.dot(dsb, k2, preferred_element_type=F32)
                dk_s[cols, window] += lax.dot_general(qh, dsb, tn, preferred_element_type=F32)
                dv_s[cols, window] += lax.dot_general(dob, p.astype(_CD), tn, preferred_element_type=F32)
        for i in range(4):
            rows = slice(i * DIL_W, (i + 1) * DIL_W)
            done = slice((i + 1) * DIL_W, (i + 2) * DIL_W)
            nat = _dil_rows(4 * n + i, S, dilation)
            dkb, dvb = dk_s[:, done].T, dv_s[:, done].T
            for half in range(DIL_OUT // LANES):
                cols = slice(half * LANES, (half + 1) * LANES)
                dqn_ref[half, nat, :] = dq_ref[rows, cols]
                dkn_ref[half, nat, :] = dkb[:, cols]
                dvn_ref[half, nat, :] = dvb[:, cols]
        dk_s[:, CHUNK:] = dk_s[:, :DIL_W]
        dv_s[:, CHUNK:] = dv_s[:, :DIL_W]

    def cur(which):
        return pl.BlockSpec((None, CHUNK, DIL_OUT), lambda s: (which, nchunk - 1 - s, 0))

    def prev(which):
        return pl.BlockSpec((None, DIL_W, DIL_OUT), lambda s: (which, jnp.maximum(4 * (nchunk - 1 - s) - 1, 0), 0))

    whole = pl.BlockSpec(SPLIT(S), lambda s: (0, 0, 0))
    shp = jax.ShapeDtypeStruct(SPLIT(S), F32)
    tile = pltpu.VMEM((CHUNK, DIL_OUT), F32)
    return _pcall(body, name=name, grid=(nchunk,),
                  in_specs=[cur(0), cur(1), cur(2), prev(1), prev(2), whole, whole],
                  out_specs=[whole, whole, whole], out_shape=[shp, shp, shp],
                  scratch_shapes=[pltpu.VMEM((DIL_OUT, CHUNK + DIL_W), F32), pltpu.VMEM((DIL_OUT, CHUNK + DIL_W), F32),
                                  tile, tile, tile],
                  compiler_params=pltpu.CompilerParams(dimension_semantics=("arbitrary",),
                                                       vmem_limit_bytes=VMEM_LIMIT_RESIDENT))(
        qkv, qkv, qkv, qkv, qkv, stats, do)


def _dil_mix_fwd(os_, ls_, *, name, tm=512):
    nh, S, _ = os_[0].shape

    def body(o0, o1, o2, l0, l1, l2, out_ref):
        for half in range(nh):
            ls = [l0[half], l1[half], l2[half]]
            m = jnp.maximum(jnp.maximum(ls[0], ls[1]), ls[2])
            es = [jnp.exp(l - m) for l in ls]
            den = es[0] + es[1] + es[2]
            mixed = (es[0] * o0[half] + es[1] * o1[half] + es[2] * o2[half]) / den
            out_ref[:, half * LANES:(half + 1) * LANES] = mixed.astype(out_ref.dtype)

    halves = pl.BlockSpec((nh, tm, LANES), lambda i: (0, i, 0))
    row = pl.BlockSpec((tm, nh * LANES), lambda i: (i, 0))
    return _pcall(body, name=name, grid=(S // tm,), in_specs=[halves] * 6, out_specs=row,
                  out_shape=jax.ShapeDtypeStruct((S, nh * LANES), _CD), compiler_params=_params("parallel"))(*os_, *ls_)


def _dil_mix_bwd(doa, os_, ls_, *, name, tm=512, after=None):
    nh, S, _ = os_[0].shape

    def body(d_ref, o0, o1, o2, l0, l1, l2, do0, do1, do2, st0, st1, st2):
        first = lax.broadcasted_iota(jnp.int32, (tm, LANES), 1) % HEAD_DIM < STAT_OFFSET
        for half in range(nh):
            dv = d_ref[:, half * LANES:(half + 1) * LANES]
            ls = [l0[half], l1[half], l2[half]]
            m = jnp.maximum(jnp.maximum(ls[0], ls[1]), ls[2])
            es = [jnp.exp(l - m) for l in ls]
            den = es[0] + es[1] + es[2]
            al = [e / den for e in es]
            da = [_head_sum(dv * o[half]) for o in (o0, o1, o2)]
            mean = al[0] * da[0] + al[1] * da[1] + al[2] * da[2]
            for a, l, do_ref, st_ref in zip(al, ls, (do0, do1, do2), (st0, st1, st2)):
                do_ref[half] = a * dv
                st_ref[half] = jnp.where(first, l, -a * mean)

    halves = pl.BlockSpec((nh, tm, LANES), lambda i: (0, i, 0))
    row = pl.BlockSpec((tm, nh * LANES), lambda i: (i, 0))
    shp = jax.ShapeDtypeStruct((nh, S, LANES), F32)
    return _pcall(body, after, name=name, grid=(S // tm,), in_specs=[row] + [halves] * 6, out_specs=[halves] * 6,
                  out_shape=[shp] * 6, compiler_params=_params("parallel"))(doa, *os_, *ls_)


FOX_T = 512


PACK = 2 * HEAD_DIM
HEAD_PAIRS = N_FOX_HEADS // 2
FOX_HPS = 8
Q_BLOCK0 = 0
K_BLOCK0 = FOX_WIDTH // PACK
V_BLOCK0 = 2 * FOX_WIDTH // PACK


def _pieces(x):
    hi = x.astype(jnp.bfloat16).astype(F32)
    r = x - hi
    mid = r.astype(jnp.bfloat16).astype(F32)
    lo = (r - mid).astype(jnp.bfloat16).astype(F32)
    return [hi, mid, lo]


def _extras(first, second, rows):
    lane = lax.broadcasted_iota(jnp.int32, (rows, HEAD_DIM), 1)
    out = jnp.zeros((rows, HEAD_DIM), F32)
    for base, triple in ((0, first), (3, second)):
        if all(isinstance(v, float) for v in triple) and len(set(triple)) == 1:
            if triple[0] != 0.0:
                out = jnp.where(jnp.logical_and(lane >= base, lane < base + 3), triple[0], out)
        else:
            for idx, val in enumerate(triple):
                out = jnp.where(lane == base + idx, val, out)
    return out


def _head_column(c, h):
    lane = lax.broadcasted_iota(jnp.int32, c.shape, 1)
    return jnp.sum(jnp.where(lane == h, c, 0.0), axis=1, keepdims=True)


ONES3 = [1.0, 1.0, 1.0]
ZEROS3 = [0.0, 0.0, 0.0]


def _fox_pack_fwd(qkv, c, *, name, tm=512):
    S = qkv.shape[0]

    def body(q_ref, k_ref, v_ref, c_ref, qo_ref, ko_ref, vo_ref):
        hp = pl.program_id(1)
        cv = c_ref[...]
        v_extras = jnp.where(lax.broadcasted_iota(jnp.int32, (tm, HEAD_DIM), 1) < 3, 1.0, 0.0).astype(vo_ref.dtype)
        for hh in range(2):
            ch = _pieces(_head_column(cv, 2 * hp + hh))
            src = slice(hh * HEAD_DIM, (hh + 1) * HEAD_DIM)
            lo = slice(hh * PACK, hh * PACK + HEAD_DIM)
            hi = slice(hh * PACK + HEAD_DIM, (hh + 1) * PACK)
            qo_ref[:, lo] = (q_ref[:, src].astype(F32) * ATTN_SCALE).astype(qo_ref.dtype)
            qo_ref[:, hi] = _extras(ch, ONES3, tm).astype(qo_ref.dtype)
            ko_ref[:, lo] = k_ref[:, src]
            ko_ref[:, hi] = _extras(ONES3, [-p for p in ch], tm).astype(ko_ref.dtype)
            vo_ref[:, lo] = v_ref[:, src]
            vo_ref[:, hi] = v_extras

    def src(block0):
        return pl.BlockSpec((tm, PACK), lambda i, hp: (i, block0 + hp))

    out = pl.BlockSpec((tm, 2 * PACK), lambda i, hp: (i, hp))
    shp = jax.ShapeDtypeStruct((S, N_FOX_HEADS * PACK), _CD)
    return _pcall(body, name=name, grid=(S // tm, HEAD_PAIRS),
                  in_specs=[src(Q_BLOCK0), src(K_BLOCK0), src(V_BLOCK0), pl.BlockSpec((tm, PACK), lambda i, hp: (i, 0))],
                  out_specs=[out, out, out], out_shape=[shp, shp, shp],
                  compiler_params=_params("parallel", "parallel"))(qkv, qkv, qkv, c)


def _fox_fwd(qp, kp, vp, *, name):
    S = qp.shape[0]
    nt = S // FOX_T
    nt_dims = (((1,), (1,)), ((), ()))
    tn_dims = (((0,), (0,)), ((), ()))

    def body(i_tab, j_tab, q_ref, k_ref, v_ref, o_ref, l_ref, m_s, acc_s):
        t = pl.program_id(1)
        i, j = i_tab[t], j_tab[t]

        @pl.when(j == 0)
        def _():
            m_s[...] = jnp.full((FOX_HPS, 1, FOX_T), NEG_INF, F32)
            acc_s[...] = jnp.zeros((FOX_HPS, PACK, FOX_T), F32)

        def tile(diagonal):
            for hh in range(FOX_HPS):
                cols = slice(hh * PACK, (hh + 1) * PACK)
                st = lax.dot_general(k_ref[:, cols], q_ref[:, cols], nt_dims, preferred_element_type=F32)
                if diagonal:
                    key = lax.broadcasted_iota(jnp.int32, (FOX_T, FOX_T), 0)
                    qry = lax.broadcasted_iota(jnp.int32, (FOX_T, FOX_T), 1)
                    st = jnp.where(key <= qry, st, NEG_INF)
                m_old = m_s[hh]
                m_new = jnp.maximum(m_old, jnp.max(st, axis=0, keepdims=True))
                pt = jnp.exp(st - m_new)
                acc_s[hh] = jnp.exp(m_old - m_new) * acc_s[hh] + lax.dot_general(
                    v_ref[:, cols], pt.astype(_CD), tn_dims, preferred_element_type=F32)
                m_s[hh] = m_new

        @pl.when(j < i)
        def _():
            tile(False)

        @pl.when(j == i)
        def _():
            tile(True)
            for hh in range(FOX_HPS):
                acc = acc_s[hh]
                den = acc[HEAD_DIM:HEAD_DIM + 1, :]
                cols = slice(hh * HEAD_DIM, (hh + 1) * HEAD_DIM)
                o_ref[:, cols] = (acc[:HEAD_DIM, :] / den).T
                l_ref[:, cols] = jnp.broadcast_to(m_s[hh] + jnp.log(den), (HEAD_DIM, FOX_T)).T

    pairs = [(i, j) for i in range(nt) for j in range(i + 1)]
    i_tab = jnp.asarray([p[0] for p in pairs], jnp.int32)
    j_tab = jnp.asarray([p[1] for p in pairs], jnp.int32)
    qs = pl.BlockSpec((FOX_T, FOX_HPS * PACK), lambda hp, t, it, jt: (it[t], hp))
    ks = pl.BlockSpec((FOX_T, FOX_HPS * PACK), lambda hp, t, it, jt: (jt[t], hp))
    os_ = pl.BlockSpec((FOX_T, FOX_HPS * HEAD_DIM), lambda hp, t, it, jt: (it[t], hp))
    shp = jax.ShapeDtypeStruct((S, FOX_WIDTH), F32)
    grid_spec = pltpu.PrefetchScalarGridSpec(
        num_scalar_prefetch=2, grid=(N_FOX_HEADS // FOX_HPS, len(pairs)), in_specs=[qs, ks, ks], out_specs=[os_, os_],
        scratch_shapes=[pltpu.VMEM((FOX_HPS, 1, FOX_T), F32), pltpu.VMEM((FOX_HPS, PACK, FOX_T), F32)])
    return _pcall(body, name=name, grid_spec=grid_spec, out_shape=[shp, shp],
                  compiler_params=_params("parallel", "arbitrary"))(i_tab, j_tab, qp, kp, vp)


def _fox_pack_bwd(qkv, c, o, lse, do, *, name, tm=512, after=None):
    S = qkv.shape[0]

    def body(q_ref, c_ref, o_ref, l_ref, do_ref, qo_ref, do_out_ref):
        hp = pl.program_id(1)
        cv = c_ref[...]
        for hh in range(2):
            src = slice(hh * HEAD_DIM, (hh + 1) * HEAD_DIM)
            lo = slice(hh * PACK, hh * PACK + HEAD_DIM)
            hi = slice(hh * PACK + HEAD_DIM, (hh + 1) * PACK)
            shift = _head_column(cv, 2 * hp + hh) - l_ref[:, hh * HEAD_DIM:hh * HEAD_DIM + 1]
            dov = do_ref[:, src]
            dsum = jnp.sum(dov * o_ref[:, src], axis=-1, keepdims=True)
            qo_ref[:, lo] = (q_ref[:, src].astype(F32) * ATTN_SCALE).astype(qo_ref.dtype)
            qo_ref[:, hi] = _extras(_pieces(shift), ONES3, tm).astype(qo_ref.dtype)
            do_out_ref[:, lo] = dov.astype(do_out_ref.dtype)
            do_out_ref[:, hi] = _extras(_pieces(-dsum), ZEROS3, tm).astype(do_out_ref.dtype)

    pair = pl.BlockSpec((tm, PACK), lambda i, hp: (i, hp))
    out = pl.BlockSpec((tm, 2 * PACK), lambda i, hp: (i, hp))
    shp = jax.ShapeDtypeStruct((S, N_FOX_HEADS * PACK), _CD)
    return _pcall(body, after, name=name, grid=(S // tm, HEAD_PAIRS),
                  in_specs=[pl.BlockSpec((tm, PACK), lambda i, hp: (i, Q_BLOCK0 + hp)),
                            pl.BlockSpec((tm, PACK), lambda i, hp: (i, 0)), pair, pair, pair],
                  out_specs=[out, out], out_shape=[shp, shp],
                  compiler_params=_params("parallel", "parallel"))(qkv, c, o, lse, do)


def _fox_bwd(qp, kp, vp, dop, *, name):
    S = qp.shape[0]
    nt = S // FOX_T
    nt_dims = (((1,), (1,)), ((), ()))
    tn_dims = (((0,), (0,)), ((), ()))

    def body(i_tab, j_tab, q_ref, k_ref, v_ref, do_ref, dq_ref, dk_ref, dv_ref, dc_ref, dr_ref,
             dq_s, dk_s, dv_s, dc_s, dr_s):
        t = pl.program_id(1)
        i, j = i_tab[t], j_tab[t]

        @pl.when(t == 0)
        def _():
            dq_s[...] = jnp.zeros((S, FOX_HPS * PACK), F32)
            dr_s[...] = jnp.zeros((FOX_HPS, 1, S), F32)

        @pl.when(i == j)
        def _():
            dk_s[...] = jnp.zeros((FOX_T, FOX_HPS * PACK), F32)
            dv_s[...] = jnp.zeros((FOX_T, FOX_HPS * PACK), F32)
            dc_s[...] = jnp.zeros((FOX_HPS, FOX_T, 1), F32)

        def tile(diagonal):
            rows = pl.ds(pl.multiple_of(i * FOX_T, FOX_T), FOX_T)
            for hh in range(FOX_HPS):
                cols = slice(hh * PACK, (hh + 1) * PACK)
                qv, kv, vv, dov = q_ref[:, cols], k_ref[:, cols], v_ref[:, cols], do_ref[:, cols]
                pt = jnp.exp(lax.dot_general(kv, qv, nt_dims, preferred_element_type=F32))
                if diagonal:
                    key = lax.broadcasted_iota(jnp.int32, (FOX_T, FOX_T), 0)
                    qry = lax.broadcasted_iota(jnp.int32, (FOX_T, FOX_T), 1)
                    pt = jnp.where(key <= qry, pt, 0.0)
                dst = pt * lax.dot_general(vv, dov, nt_dims, preferred_element_type=F32)
                dsb = dst.astype(_CD)
                dc_s[hh] += jnp.sum(dst, axis=1, keepdims=True)
                dr_s[hh, :, rows] += jnp.sum(dst, axis=0, keepdims=True)
                dv_s[:, cols] += jnp.dot(pt.astype(_CD), dov, preferred_element_type=F32)
                dk_s[:, cols] += jnp.dot(dsb, qv, preferred_element_type=F32)
                dq_s[rows, cols] += lax.dot_general(dsb, kv, tn_dims, preferred_element_type=F32)

        @pl.when(i > j)
        def _():
            tile(False)

        @pl.when(i == j)
        def _():
            tile(True)

        @pl.when(i == nt - 1)
        def _():
            for hh in range(FOX_HPS):
                src = slice(hh * PACK, hh * PACK + HEAD_DIM)
                dst_cols = slice(hh * HEAD_DIM, (hh + 1) * HEAD_DIM)
                dk_ref[:, dst_cols] = dk_s[:, src].astype(dk_ref.dtype)
                dv_ref[:, dst_cols] = dv_s[:, src].astype(dv_ref.dtype)
                dc_ref[:, dst_cols] = jnp.broadcast_to(dc_s[hh], (FOX_T, HEAD_DIM))

        @pl.when(t == len(pairs) - 1)
        def _():
            for hh in range(FOX_HPS):
                dq_ref[:, hh * HEAD_DIM:(hh + 1) * HEAD_DIM] = (
                    dq_s[:, hh * PACK:hh * PACK + HEAD_DIM] * ATTN_SCALE).astype(dq_ref.dtype)
            dr_ref[...] = dr_s[...]

    pairs = [(i, j) for j in range(nt) for i in range(j, nt)]
    i_tab = jnp.asarray([p[0] for p in pairs], jnp.int32)
    j_tab = jnp.asarray([p[1] for p in pairs], jnp.int32)
    wide, narrow = FOX_HPS * PACK, FOX_HPS * HEAD_DIM
    qs = pl.BlockSpec((FOX_T, wide), lambda hp, t, it, jt: (it[t], hp))
    ks = pl.BlockSpec((FOX_T, wide), lambda hp, t, it, jt: (jt[t], hp))
    whole = pl.BlockSpec((S, narrow), lambda hp, t, it, jt: (0, hp))
    cs = pl.BlockSpec((FOX_T, narrow), lambda hp, t, it, jt: (jt[t], hp))
    rs = pl.BlockSpec((FOX_HPS, 1, S), lambda hp, t, it, jt: (hp, 0, 0))
    shp = jax.ShapeDtypeStruct((S, FOX_WIDTH), _CD)
    grid_spec = pltpu.PrefetchScalarGridSpec(
        num_scalar_prefetch=2, grid=(N_FOX_HEADS // FOX_HPS, len(pairs)), in_specs=[qs, ks, ks, qs],
        out_specs=[whole, cs, cs, cs, rs],
        scratch_shapes=[pltpu.VMEM((S, wide), F32), pltpu.VMEM((FOX_T, wide), F32),
                        pltpu.VMEM((FOX_T, wide), F32), pltpu.VMEM((FOX_HPS, FOX_T, 1), F32),
                        pltpu.VMEM((FOX_HPS, 1, S), F32)])
    return _pcall(body, name=name, grid_spec=grid_spec,
                  out_shape=[shp, shp, shp, jax.ShapeDtypeStruct((S, FOX_WIDTH), F32),
                             jax.ShapeDtypeStruct((N_FOX_HEADS, 1, S), F32)],
                  compiler_params=_params("parallel", "arbitrary"))(i_tab, j_tab, qp, kp, vp, dop)


def _layer_step(x, tgt, w, p, late_weights=None, grad_sink=None, after=None, first_weights=None):
    S = x.shape[0]
    after_norm, after_proj = after if after is not None else (None, None)
    h = _rms_fwd(x, p["norm_mix_g"], name="rms_mix", after=after_norm)
    if first_weights is not None:
        w = {**w, **first_weights(h)}
    qkv = _mm(h, w["qkv"][:, 3 * DIL_WIDTH:], name="proj_fox", out_dtype=_CD, tn=768, tm=2048, after=after_proj)
    dil_qkv = _proj_dil(h, w["qkv"], name="proj_dil")
    zf = _mm(h, w["f"], name="proj_f")
    gl = _mm(h, w["g"], name="proj_gate", tn=1024, out_dtype=_CD)

    dil_o, dil_l = [], []
    for g in range(N_DIL_GROUPS):
        og, lg = _dil_fwd(dil_qkv[g], g, name=f"dil_fwd{g}")
        dil_o.append(og), dil_l.append(lg)
    o_a = _dil_mix_fwd(dil_o, dil_l, name="dil_mix")

    c = _fox_cumsum(zf, p["b_fgt"], name="fox_cumsum")
    fqp, fkp, fvp = _fox_pack_fwd(qkv, c, name="fox_pack")
    o_b, flse = _fox_fwd(fqp, fkp, fvp, name="fox_fwd")

    if late_weights is not None:
        w = {**w, **late_weights(o_b)}
    y_a = _mm(o_a, w["dil_out"], name="y_a", tn=1024, out_dtype=_CD)
    y_b = _mm(o_b, w["fox_out"], name="y_b", tn=1024, out_dtype=_CD)
    merged = _gate_fwd(gl, p["b_gate"], y_a, y_b, name="gate_fwd")
    x1 = _mm(merged, w["out"], name="mix_out", add=x)

    h2 = _rms_fwd(x1, p["norm_ffn_g"], name="rms_ffn")
    gate, up, act = _ffn_in_act(h2, w["ffn_in"], name="ffn_in")
    x2 = _mm(act, w["ffn_down"], name="ffn_down", add=x1, tk=2816)

    loss, dx2, dg_final = _loss_head(x2, p["norm_final_g"], tgt, name="loss_head")

    gw_ffn_down = _mm(act, dx2, name="gw_ffn_down", ta=True, out_dtype=_CD, tm=1408)
    dgu = _d_swiglu(dx2, w["ffn_down"], gate, up, name="d_swiglu")
    dh2 = _mm(dgu, w["ffn_in"], name="d_h2", tb=True, tk=1408, b_blocks=True, tm=2048, a_halves=True)
    gw_ffn_in = _mm(h2, dgu, name="gw_ffn_in", ta=True, out_dtype=_CD, tn=1408, out_blocks=1408, b_halves=True)
    sink = grad_sink if grad_sink is not None else (lambda group, grads: None)
    tok = sink("ffn", dict(ffn_in=gw_ffn_in, ffn_down=gw_ffn_down))
    dx1, dg_ffn = _rms_bwd(x1, p["norm_ffn_g"], dh2, dx2, name="rms_ffn_bwd", after=tok)

    dmerged = _mm(dx1, w["out"], name="d_merged", tb=True, out_dtype=_CD)
    gw_out = _mm(merged, dx1, name="gw_out", ta=True, out_dtype=_CD)
    dy_a, dy_b, dgl, db_gate = _gate_bwd(dmerged, gl, p["b_gate"], y_a, y_b, name="gate_bwd")
    do_a = _mm(dy_a, w["dil_out"], name="d_o_a", tb=True)
    gw_dil_out = _mm(o_a, dy_a, name="gw_dil_out", ta=True, out_dtype=_CD, tn=1024)
    do_b = _mm(dy_b, w["fox_out"], name="d_o_b", tb=True)
    gw_fox_out = _mm(o_b, dy_b, name="gw_fox_out", ta=True, out_dtype=_CD, tn=1024)
    tok = sink("mix", dict(dil_out=gw_dil_out, fox_out=gw_fox_out, out=gw_out))

    bqp, bdop = _fox_pack_bwd(qkv, c, o_b, flse, do_b, name="fox_pack_bwd", after=tok)
    dqp, dkp, dvp, dck, dcq = _fox_bwd(bqp, fkp, fvp, bdop, name="fox_bwd")
    dc = dcq[:, 0, :].T - dck.reshape(S, N_FOX_HEADS, HEAD_DIM)[:, :, 0]
    dc = jnp.pad(dc, ((0, 0), (0, F_PAD - N_FOX_HEADS)))
    dzf, db_fgt = _fox_cumsum_bwd(dc, zf, p["b_fgt"], name="fox_cumsum_bwd")

    douts = _dil_mix_bwd(do_a, dil_o, dil_l, name="dil_mix_bwd", after=tok)
    dqs, dks, dvs = [], [], []
    for g in range(N_DIL_GROUPS):
        dq, dk, dv = _dil_bwd(dil_qkv[g], douts[3 + g], douts[g], g, name=f"dil_bwd{g}")
        for parts, t in ((dqs, dq), (dks, dk), (dvs, dv)):
            parts.extend([t[0].astype(_CD), t[1].astype(_CD)])
    dqkv = jnp.concatenate(dqs + dks + dvs + [dqp, dkp, dvp], axis=1)

    gw_qkv = _mm(h, dqkv, name="gw_qkv", ta=True, out_dtype=_CD, tn=768)
    gw_g = _mm(h, dgl, name="gw_gate", ta=True, out_dtype=_CD)
    gw_f = _mm(h, dzf, name="gw_f", ta=True, out_dtype=_CD)
    tok = sink("in", dict(qkv=gw_qkv, f=gw_f, g=gw_g))
    dh = _mm(dqkv, w["qkv"], name="d_h_qkv", tb=True, tk=1920, tm=2048, after=tok)
    dh = _mm(dgl, w["g"], name="d_h_gate", tb=True, add=dh)
    dh = _mm(dzf, w["f"], name="d_h_f", tb=True, add=dh)
    dx, dg_mix = _rms_bwd(x, p["norm_mix_g"], dh, dx1, name="rms_mix_bwd")

    gw = dict(qkv=gw_qkv, f=gw_f, g=gw_g, dil_out=gw_dil_out, fox_out=gw_fox_out, out=gw_out, ffn_in=gw_ffn_in,
              ffn_down=gw_ffn_down)
    small = dict(norm_mix_g=dg_mix, b_fgt=db_fgt, b_gate=db_gate, norm_ffn_g=dg_ffn, norm_final_g=dg_final)
    return loss, dx, gw, small


def _position():
    return lax.axis_index("x"), lax.axis_index("y"), lax.axis_index("c")


def _other_chips(x, y):
    return [(1 - x, y), (x, 1 - y), (1 - x, 1 - y)]


ROW_TILE = 16


def _row_chunks(rows, want=4):
    n = want
    while n > 1 and rows % (n * ROW_TILE):
        n //= 2
    return n


SEM_SPEC = pl.BlockSpec(memory_space=pltpu.SEMAPHORE)
ANY_SPEC = pl.BlockSpec(memory_space=pl.ANY)
DATAFLOW = pltpu.SideEffectType.DATAFLOW_SIDE_EFFECTING


def _in_hbm(a):
    return pltpu.with_memory_space_constraint(a, pltpu.HBM)


def _split_copy_start(srcs, land_shapes, copies, after, *, name):
    n, m = len(srcs), len(land_shapes)

    def body(*refs):
        src_refs, land_refs = refs[:n], refs[n:n + m]
        send_sems, recv_sems = refs[n + m + 1], refs[n + m + 2]
        token = refs[-1]
        x, y, c = _position()
        for k, (src, dst, peer) in enumerate(copies(x, y, c, src_refs, land_refs)):
            pltpu.make_async_remote_copy(src_ref=src, dst_ref=dst, send_sem=send_sems.at[k], recv_sem=recv_sems.at[k],
                                         device_id=peer, device_id_type=MESH).start()
        token[...] = jnp.zeros_like(token)

    lands = [lax.empty(s.shape, s.dtype) for s in land_shapes]
    count = len(copies(0, 0, 0, srcs, lands))
    out = _pcall(
        body, name=name,
        out_shape=(pltpu.SemaphoreType.DMA((count,)), pltpu.SemaphoreType.DMA((count,)),
                   *[pltpu.HBM(s.shape, s.dtype) for s in srcs], *[pltpu.HBM(s.shape, s.dtype) for s in land_shapes],
                   jax.ShapeDtypeStruct((8, 128), F32)),
        in_specs=[HBM_SPEC] * (n + m) + [ANY_SPEC],
        out_specs=(SEM_SPEC, SEM_SPEC, *[HBM_SPEC] * (n + m), pl.BlockSpec(memory_space=pltpu.VMEM)),
        input_output_aliases={k: 2 + k for k in range(n + m)},
        compiler_params=pltpu.CompilerParams(has_side_effects=DATAFLOW),
    )(*[_in_hbm(s) for s in srcs], *[_in_hbm(l) for l in lands], after)
    return out[0], out[1], list(out[2:2 + n]), list(out[2 + n:2 + n + m]), out[-1]


def _split_copy_wait(send_sems, recv_sems, srcs, lands, copies, after, *, name):
    n, m = len(srcs), len(lands)

    def body(*refs):
        src_refs, land_refs = refs[:n], refs[n:n + m]
        send, recv = refs[n + m], refs[n + m + 1]
        x, y, c = _position()
        for k, (src, dst, peer) in enumerate(copies(x, y, c, src_refs, land_refs)):
            cp = pltpu.make_async_remote_copy(src_ref=src, dst_ref=dst, send_sem=send.at[k], recv_sem=recv.at[k],
                                              device_id=peer, device_id_type=MESH)
            cp.wait_send()
            cp.wait_recv()

    afters = list(after) if isinstance(after, (list, tuple)) else [after]
    out = _pcall(
        body, name=name,
        out_shape=tuple(pltpu.HBM(s.shape, s.dtype) for s in list(srcs) + list(lands)),
        in_specs=[HBM_SPEC] * (n + m) + [SEM_SPEC, SEM_SPEC] + [ANY_SPEC] * len(afters),
        out_specs=tuple([HBM_SPEC] * (n + m)),
        input_output_aliases={k: k for k in range(n + m)},
        compiler_params=pltpu.CompilerParams(has_side_effects=DATAFLOW),
    )(*srcs, *lands, send_sems, recv_sems, *afters)
    return list(out[:n]), list(out[n:])


def _gather_copies(x, y, c, shard_refs, land_refs):
    out = []
    for s, l in zip(shard_refs, land_refs):
        half = s.shape[0] // 2
        nq = _row_chunks(half)
        for cx, cy in _other_chips(x, y):
            for q in range(nq):
                rows = pl.ds(c * half + q * (half // nq), half // nq)
                out.append((s.at[rows, :], l.at[2 * x + y, rows, :], (cx, cy, c)))
    return out


def _gather_whole_copies(x, y, c, shard_refs, land_refs):
    out = []
    for s, l in zip(shard_refs, land_refs):
        nq = _row_chunks(s.shape[0])
        for cx, cy in _other_chips(x, y):
            for q in range(nq):
                rows = pl.ds(q * (s.shape[0] // nq), s.shape[0] // nq)
                out.append((s.at[rows, :], l.at[2 * x + y, rows, :], (cx, cy, c)))
    return out


def _scatter_copies(x, y, c, part_refs, land_refs):
    out = []
    for p, l in zip(part_refs, land_refs):
        nq = _row_chunks(p.shape[1])
        for r, (cx, cy) in enumerate(_other_chips(x, y)):
            for q in range(nq):
                rows = pl.ds(q * (p.shape[1] // nq), p.shape[1] // nq)
                out.append((p.at[2 * cx + cy, rows, :], l.at[r, rows, :], (cx, cy, c)))
    return out


def _forward_halves(lands, *, name):
    n = len(lands)

    def body(*refs):
        ins = refs[:n]
        send_sems, recv_sems = refs[2 * n:]
        x, y, c = _position()
        copies = []
        for w in range(n):
            half = ins[w].shape[1] // 2
            for r, (cx, cy) in enumerate(_other_chips(x, y)):
                blk = ins[w].at[2 * cx + cy, pl.ds(c * half, half), :]
                cp = pltpu.make_async_remote_copy(src_ref=blk, dst_ref=blk, send_sem=send_sems.at[w, r],
                                                  recv_sem=recv_sems.at[w, r], device_id=(x, y, 1 - c),
                                                  device_id_type=MESH)
                cp.start()
                copies.append(cp)
        for w in range(n):
            half = ins[w].shape[1] // 2
            for r, (cx, cy) in enumerate(_other_chips(x, y)):
                blk = ins[w].at[2 * cx + cy, pl.ds((1 - c) * half, half), :]
                pltpu.make_async_remote_copy(src_ref=blk, dst_ref=blk, send_sem=send_sems.at[w, r],
                                             recv_sem=recv_sems.at[w, r], device_id=(x, y, 1 - c),
                                             device_id_type=MESH).wait_recv()
        for cp in copies:
            cp.wait_send()

    return _pcall(
        body, name=name, in_specs=[HBM_SPEC] * n, out_specs=[HBM_SPEC] * n,
        out_shape=[jax.ShapeDtypeStruct(l.shape, l.dtype) for l in lands],
        input_output_aliases={k: k for k in range(n)},
        scratch_shapes=[pltpu.SemaphoreType.DMA((n, 3)), pltpu.SemaphoreType.DMA((n, 3))],
    )(*lands)


def _swap_halves(grads, name="swap_halves"):
    n = len(grads)

    def body(*refs):
        ins, outs = refs[:n], refs[n:2 * n]
        send_sems, recv_sems = refs[2 * n:]
        x, y, c = _position()
        copies = []
        for w in range(n):
            half = ins[w].shape[1] // 2
            cp = pltpu.make_async_remote_copy(
                src_ref=ins[w].at[:, pl.ds((1 - c) * half, half), :], dst_ref=outs[w], send_sem=send_sems.at[w],
                recv_sem=recv_sems.at[w], device_id=(x, y, 1 - c), device_id_type=MESH)
            cp.start()
            copies.append(cp)
        for cp in copies:
            cp.wait()

    return _pcall(
        body, name=name, in_specs=[HBM_SPEC] * n, out_specs=[HBM_SPEC] * n,
        out_shape=[jax.ShapeDtypeStruct((4, g.shape[1] // 2, g.shape[2]), g.dtype) for g in grads],
        scratch_shapes=[pltpu.SemaphoreType.DMA((n,)), pltpu.SemaphoreType.DMA((n,))],
    )(*grads)


def _share_halves(halves):
    n = len(halves)

    def body(*refs):
        ins, outs = refs[:n], refs[n:2 * n]
        send_sems, recv_sems = refs[2 * n:]
        x, y, c = _position()
        copies = []
        for w in range(n):
            cp = pltpu.make_async_remote_copy(src_ref=ins[w], dst_ref=outs[w], send_sem=send_sems.at[w],
                                              recv_sem=recv_sems.at[w], device_id=(x, y, 1 - c), device_id_type=MESH)
            cp.start()
            copies.append(cp)
        for cp in copies:
            cp.wait()

    return _pcall(
        body, name="share_halves", in_specs=[HBM_SPEC] * n, out_specs=[HBM_SPEC] * n,
        out_shape=[jax.ShapeDtypeStruct(h.shape, h.dtype) for h in halves],
        scratch_shapes=[pltpu.SemaphoreType.DMA((n,)), pltpu.SemaphoreType.DMA((n,))],
    )(*halves)


def _sum_small(part):
    rows, width = part.shape

    def body(x_ref, out_ref, all_ref, send_sems, recv_sems):
        x, y, c = _position()
        me, sibling = (x, y, c), (x, y, 1 - c)
        chips = _other_chips(x, y)

        def block(px, py, pc):
            return all_ref.at[pl.ds((4 * px + 2 * py + pc) * rows, rows), :]

        def copy(k, blk, to, src=None):
            return pltpu.make_async_remote_copy(
                src_ref=block(*blk) if src is None else src, dst_ref=block(*blk), send_sem=send_sems.at[k],
                recv_sem=recv_sems.at[k], device_id=to, device_id_type=MESH)

        all_ref[pl.ds((4 * x + 2 * y + c) * rows, rows), :] = x_ref[...]
        first = [copy(0, me, sibling, src=x_ref)]
        first += [copy(1 + j, me, (*chip, c), src=x_ref) for j, chip in enumerate(chips)]
        for cp in first:
            cp.start()
        passed = [copy(4 + j, (*chip, c), sibling) for j, chip in enumerate(chips)]
        for j, chip in enumerate(chips):
            copy(1 + j, (*chip, c), me).wait_recv()
            passed[j].start()
        copy(0, sibling, me).wait_recv()
        for j, chip in enumerate(chips):
            copy(4 + j, (*chip, 1 - c), me).wait_recv()
        for cp in first + passed:
            cp.wait_send()
        total = all_ref[0:rows, :]
        for d in range(1, 8):
            total = total + all_ref[d * rows:(d + 1) * rows, :]
        out_ref[...] = total

    vm = pl.BlockSpec(memory_space=pltpu.VMEM)
    return _pcall(
        body, name="sum_small", in_specs=[vm], out_specs=vm, out_shape=jax.ShapeDtypeStruct((rows, width), F32),
        scratch_shapes=[pltpu.VMEM((8 * rows, width), F32), pltpu.SemaphoreType.DMA((7,)), pltpu.SemaphoreType.DMA((7,))],
    )(part)


def _row_tile(R, C, itemsize=4, budget=1 << 20):
    for t in (512, 256, 128, 64, 32, 16, 8):
        if R % t == 0 and t * C * itemsize <= budget:
            return t
    return R


def _add_halves(g, recv, c, *, name):
    _, R, C = g.shape
    half = R // 2
    t = _row_tile(half, C)
    nb = half // t

    def body(c_ref, g_ref, r_ref, o_ref):
        o_ref[...] = (g_ref[...].astype(F32) + r_ref[...].astype(F32)).astype(o_ref.dtype)

    grid_spec = pltpu.PrefetchScalarGridSpec(
        num_scalar_prefetch=1, grid=(4, nb),
        in_specs=[pl.BlockSpec((1, t, C), lambda k, i, cr: (k, cr[0] * nb + i, 0)),
                  pl.BlockSpec((1, t, C), lambda k, i, cr: (k, i, 0))],
        out_specs=pl.BlockSpec((1, t, C), lambda k, i, cr: (k, i, 0)))
    return _pcall(body, name=name, grid_spec=grid_spec, out_shape=jax.ShapeDtypeStruct((4, half, C), g.dtype),
                  compiler_params=_params("parallel", "parallel"))(c, g, recv)


def _add_owners(mine, recv, *, name):
    half, C = mine.shape
    t = _row_tile(half, C)

    def body(m_ref, r_ref, o_ref):
        o_ref[...] = ((m_ref[...].astype(F32) + r_ref[0].astype(F32)) + r_ref[1].astype(F32)) + r_ref[2].astype(F32)

    return _pcall(body, name=name, grid=(half // t,),
                  in_specs=[pl.BlockSpec((t, C), lambda i: (i, 0)), pl.BlockSpec((3, t, C), lambda i: (0, i, 0))],
                  out_specs=pl.BlockSpec((t, C), lambda i: (i, 0)), out_shape=jax.ShapeDtypeStruct((half, C), F32),
                  compiler_params=_params("parallel"))(mine, recv)


def _adamw(w, g, m, v, *, name):
    R, C = w.shape
    t = _row_tile(R, C)
    c1 = 1.0 - ADAM_B1 ** ADAM_STEP
    c2 = 1.0 - ADAM_B2 ** ADAM_STEP

    def body(w_ref, g_ref, m_ref, v_ref, d_ref, nm_ref, nv_ref):
        gv = g_ref[...]
        mn = ADAM_B1 * m_ref[...] + (1.0 - ADAM_B1) * gv
        vn = ADAM_B2 * v_ref[...] + (1.0 - ADAM_B2) * (gv * gv)
        d_ref[...] = -ADAM_LR * ((mn / c1) / (jnp.sqrt(vn / c2) + ADAM_EPS) + ADAM_WD * w_ref[...])
        nm_ref[...] = mn
        nv_ref[...] = vn

    blk = pl.BlockSpec((t, C), lambda i: (i, 0))
    shp = jax.ShapeDtypeStruct((R, C), F32)
    return _pcall(body, name=name, grid=(R // t,), in_specs=[blk] * 4, out_specs=[blk] * 3, out_shape=[shp] * 3,
                  compiler_params=_params("parallel"))(w, g, m, v)


BIG = ("w_in", "w_dil_out", "w_fox_out", "w_out", "w_ffn_in", "w_ffn_down")
SMALL = ("norm_mix_g", "b_fgt", "b_gate", "norm_ffn_g", "norm_final_g")
ORDER = ("norm_mix_g", "w_in", "b_fgt", "b_gate", "w_dil_out", "w_fox_out", "w_out", "norm_ffn_g", "w_ffn_in",
         "w_ffn_down", "norm_final_g")
SMALL_ROWS = {"norm_mix_g": (0, 1), "b_gate": (1, 3), "norm_ffn_g": (3, 4), "norm_final_g": (4, 5), "b_fgt": (5, 6)}


def _columns_to_blocks(full, ncol):
    K = full.shape[0]
    return full.reshape(K, 4, ncol).transpose(1, 0, 2)


def _blocks_to_columns(blocks):
    n, K, ncol = blocks.shape
    return blocks.transpose(1, 0, 2).reshape(K, n * ncol)


def kernel(x, norm_mix_g, w_in, b_fgt, b_gate, w_dil_out, w_fox_out, w_out, norm_ffn_g, w_ffn_in, w_ffn_down, norm_final_g, loss_target, m_norm_mix_g, m_w_in, m_b_fgt, m_b_gate, m_w_dil_out, m_w_fox_out, m_w_out, m_norm_ffn_g, m_w_ffn_in, m_w_ffn_down, m_norm_final_g, v_norm_mix_g, v_w_in, v_b_fgt, v_b_gate, v_w_dil_out, v_w_fox_out, v_w_out, v_norm_ffn_g, v_w_ffn_in, v_w_ffn_down, v_norm_final_g):
    weights = dict(norm_mix_g=norm_mix_g, w_in=w_in, b_fgt=b_fgt, b_gate=b_gate, w_dil_out=w_dil_out,
                   w_fox_out=w_fox_out, w_out=w_out, norm_ffn_g=norm_ffn_g, w_ffn_in=w_ffn_in, w_ffn_down=w_ffn_down,
                   norm_final_g=norm_final_g)
    m_in = dict(norm_mix_g=m_norm_mix_g, w_in=m_w_in, b_fgt=m_b_fgt, b_gate=m_b_gate, w_dil_out=m_w_dil_out,
                w_fox_out=m_w_fox_out, w_out=m_w_out, norm_ffn_g=m_norm_ffn_g, w_ffn_in=m_w_ffn_in,
                w_ffn_down=m_w_ffn_down, norm_final_g=m_norm_final_g)
    v_in = dict(norm_mix_g=v_norm_mix_g, w_in=v_w_in, b_fgt=v_b_fgt, b_gate=v_b_gate, w_dil_out=v_w_dil_out,
                w_fox_out=v_w_fox_out, w_out=v_w_out, norm_ffn_g=v_norm_ffn_g, w_ffn_in=v_w_ffn_in,
                w_ffn_down=v_w_ffn_down, norm_final_g=v_norm_final_g)
    c = lax.axis_index("c")
    chip = 2 * lax.axis_index("x") + lax.axis_index("y")

    shards = {n: weights[n][0].astype(_CD) for n in BIG}
    in_shape = jax.ShapeDtypeStruct((4,) + shards["w_in"].shape, _CD)
    send_i, recv_i, in_src, in_land, token_in = _split_copy_start(
        [shards["w_in"]], [in_shape], _gather_copies, norm_mix_g, name="gather_in_start")
    late = BIG[1:]
    send_g, recv_g, late_src, late_land, token = _split_copy_start(
        [shards[n] for n in late], [jax.ShapeDtypeStruct((4,) + shards[n].shape, _CD) for n in late],
        _gather_whole_copies, token_in, name="gather_late_start")
    adam_in = [t[0] + token_in[0, 0] for t in (w_in, m_w_in, v_w_in)]
    p = dict(norm_mix_g=norm_mix_g, b_fgt=jnp.pad(b_fgt, ((0, 0), (0, F_PAD - N_FOX_HEADS))), b_gate=b_gate,
             norm_ffn_g=norm_ffn_g, norm_final_g=norm_final_g.reshape(1, D_MODEL))

    def first_weights(after):
        own, lands = _split_copy_wait(send_i, recv_i, in_src, in_land, _gather_copies, [after] + adam_in,
                                      name="gather_in_wait")
        (g_in,) = _forward_halves(lands, name="gather_in_forward")
        full_in = _blocks_to_columns(lax.dynamic_update_index_in_dim(g_in, own[0], chip, 0))
        o3 = QKV_COLS
        o4 = o3 + N_FOX_HEADS
        return dict(qkv=full_in[:, :o3], f=jnp.pad(full_in[:, o3:o4], ((0, 0), (0, F_PAD - N_FOX_HEADS))),
                    g=full_in[:, o4:])

    def late_weights(after):
        own, lands = _split_copy_wait(send_g, recv_g, late_src, late_land, _gather_whole_copies, after,
                                      name="gather_late_wait")
        g_dil, g_fox, g_out, g_ffn_in, g_ffn_down = [
            lax.dynamic_update_index_in_dim(l, s, chip, 0) for l, s in zip(lands, own)]
        return dict(dil_out=_blocks_to_columns(g_dil), fox_out=_blocks_to_columns(g_fox),
                    out=g_out.reshape(D_MODEL, D_MODEL), ffn_in=g_ffn_in,
                    ffn_down=g_ffn_down.reshape(D_FF, D_MODEL))

    c_arr = jnp.reshape(c, (1,)).astype(jnp.int32)

    def to_blocks(n, full):
        shape = weights[n].shape
        if full.ndim == 3:
            return full
        if n in ("w_out", "w_ffn_down"):
            return full.reshape(4, shape[1], shape[2])
        return _columns_to_blocks(full, shape[2])

    def pair_sums(group, named):
        names = list(named)
        blocks = [to_blocks(n, named[n]) for n in names]
        from_sibling = _swap_halves(blocks, name=f"swap_halves_{group}")
        return [_add_halves(b, r, c_arr, name=f"add_halves_{n}") for b, r, n in zip(blocks, from_sibling, names)]

    in_flight = {}

    def grad_sink(group, gw):
        if group == "in":
            named = {"w_in": jnp.concatenate([gw["qkv"], gw["f"][:, :N_FOX_HEADS], gw["g"]], axis=1)}
        else:
            named = {"w_" + k: v for k, v in gw.items()}
        sums = pair_sums(group, named)
        started = _split_copy_start(sums, [jax.ShapeDtypeStruct((3,) + s.shape[1:], s.dtype) for s in sums],
                                    _scatter_copies, next(iter(gw.values())), name=f"scatter_{group}_start")
        in_flight[group] = (list(named), started)
        return started[-1]

    loss_part, grad_x, gw, small = _layer_step(x[0], loss_target[0], {}, p, late_weights, grad_sink,
                                               (token_in, token), first_weights)

    def owner_sums(names, sums, from_chips):
        return {n: _add_owners(lax.dynamic_index_in_dim(s, chip, 0, keepdims=False), r, name=f"add_owners_{n}")
                for n, s, r in zip(names, sums, from_chips)}

    halves = {}
    for group, (names, (send_s, recv_s, srcs, lands, _)) in in_flight.items():
        sums, from_chips = _split_copy_wait(send_s, recv_s, srcs, lands, _scatter_copies, grad_x,
                                            name=f"scatter_{group}_wait")
        halves.update(owner_sums(names, sums, from_chips))
    halves = [halves[n] for n in BIG]
    grads = {}
    for n, own, other in zip(BIG, halves, _share_halves(halves)):
        pair = jnp.stack([own, other])
        grads[n] = jnp.where(c == 0, pair, pair[::-1]).reshape(2 * own.shape[0], own.shape[1])

    packed = jnp.concatenate([
        small["norm_mix_g"], small["b_gate"].reshape(2, D_MODEL), small["norm_ffn_g"], small["norm_final_g"],
        jnp.pad(small["b_fgt"], ((0, 0), (0, D_MODEL - F_PAD))), jnp.pad(loss_part, ((0, 0), (0, D_MODEL - 1))),
        jnp.zeros((1, D_MODEL), F32)], axis=0)
    summed = _sum_small(packed)
    for n in SMALL:
        lo, hi = SMALL_ROWS[n]
        grads[n] = summed[lo:hi].reshape(1, -1)[:, :weights[n].size]
    loss = summed[6, 0]

    out_g, out_d, out_m, out_v = {}, {}, {}, {}
    for n in ORDER:
        shape = weights[n].shape
        two_d = shape[1:] if len(shape) == 3 else (1, weights[n].size)
        g2 = grads[n].reshape(two_d)
        wmv = adam_in if n == "w_in" else [t.reshape(two_d) for t in (weights[n], m_in[n], v_in[n])]
        d2, m2, v2 = _adamw(wmv[0], g2, wmv[1], wmv[2], name=f"adamw_{n}")
        out_g[n], out_d[n], out_m[n], out_v[n] = (g2.reshape(shape), d2.reshape(shape), m2.reshape(shape),
                                                  v2.reshape(shape))
    return (loss, grad_x[None], *[out_g[n] for n in ORDER], *[out_d[n] for n in ORDER],
            *[out_m[n] for n in ORDER], *[out_v[n] for n in ORDER])
```

```python
import numpy as np
import jax
import jax.numpy as jnp
from jax import lax
from jax.experimental import pallas as pl
from jax.experimental.pallas import tpu as pltpu

F32 = jnp.float32
_CD = jnp.bfloat16

D_MODEL = 1024
HEAD_DIM = 64
DIL_PAIRS = ((128, 1), (512, 4), (2048, 16))
N_DIL_GROUPS = 3
DIL_HEADS = 4
DIL_W = 128
DIL_OUT = DIL_HEADS * HEAD_DIM
DIL_WIDTH = N_DIL_GROUPS * DIL_OUT
N_FOX_HEADS = 8
FOX_WIDTH = N_FOX_HEADS * HEAD_DIM
D_FF = 2816
QKV_COLS = 3 * DIL_WIDTH + 3 * FOX_WIDTH
F_PAD = 128
RMS_EPS = 1e-6
NEG_INF = -1e30
ATTN_SCALE = HEAD_DIM ** -0.5
ADAM_LR, ADAM_B1, ADAM_B2, ADAM_EPS, ADAM_WD, ADAM_STEP = 0.001, 0.9, 0.999, 1e-08, 0.01, 10

VMEM_LIMIT = 48 * 1024 * 1024
VMEM_LIMIT_RESIDENT = 56 * 1024 * 1024
LANES = 128
MESH = pl.DeviceIdType.MESH
HBM_SPEC = pl.BlockSpec(memory_space=pltpu.HBM)


def _pcall(body, after=None, **kw):
    if after is None:
        return pl.pallas_call(body, **kw)
    n_in = len(kw["in_specs"])
    kw["in_specs"] = list(kw["in_specs"]) + [pl.BlockSpec(memory_space=pl.ANY)]

    def tied(*refs):
        return body(*refs[:n_in], *refs[n_in + 1:])

    call = pl.pallas_call(tied, **kw)
    return lambda *args: call(*args, after)


def _params(*sem):
    return pltpu.CompilerParams(dimension_semantics=sem, vmem_limit_bytes=VMEM_LIMIT)


def _pick(dim, pref):
    t = (min(pref, dim) // 128) * 128
    while t >= 128:
        if dim % t == 0:
            return t
        t -= 128
    return dim


def _mm(a, b, *, name, ta=False, tb=False, out_dtype=F32, add=None, tm=1024, tn=512, tk=2048, after=None,
        b_blocks=False, out_blocks=None, a_halves=False, b_halves=False):
    if a_halves:
        M, K = a.shape[1], 2 * a.shape[2]
    elif ta:
        K, M = a.shape
    else:
        M, K = a.shape
    if b_halves:
        b_rows, b_cols = b.shape[1], 2 * b.shape[2]
    else:
        b_rows, b_cols = (b.shape[1], b.shape[0] * b.shape[2]) if b_blocks else b.shape
    if tb:
        N, K2 = b_rows, b_cols
    else:
        K2, N = b_rows, b_cols
    assert K == K2, (a.shape, b.shape)
    shard = b.shape[2] if b_blocks else None
    tm = _pick(M, tm)
    tn = _pick(shard if (b_blocks and not tb) else (out_blocks or N), tn)
    tk = _pick(shard if (b_blocks and tb) else K, tk)
    nk = K // tk
    dn = (((0 if ta else 1,), (1 if tb else 0,)), ((), ()))
    has_add = add is not None
    assert not (has_add and out_blocks)

    def body(*refs):
        a_ref, b_ref = refs[0], refs[1]
        add_ref = refs[2] if has_add else None
        o_ref = refs[3] if has_add else refs[2]
        bv = b_ref[0] if b_blocks else b_ref[...]
        p = lax.dot_general(a_ref[...].astype(_CD), bv.astype(_CD), dn, preferred_element_type=F32)

        def finish(r):
            if has_add:
                r = r + add_ref[...]
            if out_blocks:
                o_ref[0] = r.astype(out_dtype)
            else:
                o_ref[...] = r.astype(out_dtype)

        if nk == 1:
            finish(p)
        else:
            acc_ref = refs[-1]
            k = pl.program_id(2)

            @pl.when(k == 0)
            def _():
                acc_ref[...] = p

            @pl.when(k > 0)
            def _():
                acc_ref[...] += p

            @pl.when(k == nk - 1)
            def _():
                finish(acc_ref[...])

    if a_halves:
        ka = (K // 2) // tk
        a_spec = pl.BlockSpec((None, tm, tk), lambda i, j, k: (k // ka, i, k % ka))
    else:
        a_spec = pl.BlockSpec((tk, tm), lambda i, j, k: (k, i)) if ta else pl.BlockSpec((tm, tk), lambda i, j, k: (i, k))
    if b_halves:
        nb_ = (N // 2) // tn
        b_spec = pl.BlockSpec((None, tk, tn), lambda i, j, k: (j // nb_, k, j % nb_))
    elif b_blocks and tb:
        per = shard // tk
        b_spec = pl.BlockSpec((1, tn, tk), lambda i, j, k: (k // per, j, k % per))
    elif b_blocks:
        per = shard // tn
        b_spec = pl.BlockSpec((1, tk, tn), lambda i, j, k: (j // per, k, j % per))
    else:
        b_spec = pl.BlockSpec((tn, tk), lambda i, j, k: (j, k)) if tb else pl.BlockSpec((tk, tn), lambda i, j, k: (k, j))
    if out_blocks:
        oper = out_blocks // tn
        o_spec = pl.BlockSpec((1, tm, tn), lambda i, j, k: (j // oper, i, j % oper))
        out_shape = jax.ShapeDtypeStruct((N // out_blocks, M, out_blocks), out_dtype)
    else:
        o_spec = pl.BlockSpec((tm, tn), lambda i, j, k: (i, j))
        out_shape = jax.ShapeDtypeStruct((M, N), out_dtype)
    in_specs = [a_spec, b_spec] + ([o_spec] if has_add else [])
    args = (a, b) + ((add,) if has_add else ())
    return _pcall(
        body, after, name=name, grid=(M // tm, N // tn, nk), in_specs=in_specs, out_specs=o_spec,
        out_shape=out_shape,
        scratch_shapes=[pltpu.VMEM((tm, tn), F32)] if nk > 1 else [],
        compiler_params=_params("parallel", "parallel", "arbitrary"),
    )(*args)


def _rms_fwd(x, g, *, name, tm=512, after=None):
    S, D = x.shape

    def body(x_ref, g_ref, h_ref):
        xv = x_ref[...]
        r = lax.rsqrt(jnp.mean(xv * xv, axis=-1, keepdims=True) + RMS_EPS)
        h_ref[...] = ((xv * r) * g_ref[...]).astype(h_ref.dtype)

    row = pl.BlockSpec((tm, D), lambda i: (i, 0))
    return _pcall(body, after, name=name, grid=(S // tm,), in_specs=[row, pl.BlockSpec((1, D), lambda i: (0, 0))],
                  out_specs=row, out_shape=jax.ShapeDtypeStruct((S, D), _CD), compiler_params=_params("parallel"))(x, g)


def _rms_bwd(x, g, dh, dres, *, name, tm=512, after=None):
    S, D = x.shape

    def body(x_ref, g_ref, dh_ref, dres_ref, dx_ref, dg_ref):
        xv = x_ref[...]
        r = lax.rsqrt(jnp.mean(xv * xv, axis=-1, keepdims=True) + RMS_EPS)
        xh = xv * r
        dhv = dh_ref[...]
        dxh = dhv * g_ref[...]
        dx_ref[...] = dres_ref[...] + r * (dxh - xh * jnp.mean(dxh * xh, axis=-1, keepdims=True))
        part = jnp.sum(dhv * xh, axis=0, keepdims=True)

        @pl.when(pl.program_id(0) == 0)
        def _():
            dg_ref[...] = part

        @pl.when(pl.program_id(0) > 0)
        def _():
            dg_ref[...] += part

    row = pl.BlockSpec((tm, D), lambda i: (i, 0))
    vec = pl.BlockSpec((1, D), lambda i: (0, 0))
    return _pcall(body, after, name=name, grid=(S // tm,), in_specs=[row, vec, row, row], out_specs=[row, vec],
                  out_shape=[jax.ShapeDtypeStruct((S, D), F32), jax.ShapeDtypeStruct((1, D), F32)],
                  compiler_params=_params("arbitrary"))(x, g, dh, dres)


def _loss_head(x, g, tgt, *, name, tm=512):
    S, D = x.shape

    def body(x_ref, g_ref, t_ref, loss_ref, dx_ref, dg_ref):
        xv = x_ref[...]
        gv = g_ref[...]
        r = lax.rsqrt(jnp.mean(xv * xv, axis=-1, keepdims=True) + RMS_EPS)
        xh = xv * r
        err = xh * gv - t_ref[...]
        lpart = 0.5 * jnp.sum(jnp.mean(err * err, axis=-1, keepdims=True), axis=0, keepdims=True)
        dy = err * (1.0 / D)
        dxh = dy * gv
        dx_ref[...] = r * (dxh - xh * jnp.mean(dxh * xh, axis=-1, keepdims=True))
        gpart = jnp.sum(dy * xh, axis=0, keepdims=True)

        @pl.when(pl.program_id(0) == 0)
        def _():
            loss_ref[...] = lpart
            dg_ref[...] = gpart

        @pl.when(pl.program_id(0) > 0)
        def _():
            loss_ref[...] += lpart
            dg_ref[...] += gpart

    row = pl.BlockSpec((tm, D), lambda i: (i, 0))
    vec = pl.BlockSpec((1, D), lambda i: (0, 0))
    one = pl.BlockSpec((1, 1), lambda i: (0, 0))
    return _pcall(body, name=name, grid=(S // tm,), in_specs=[row, vec, row], out_specs=[one, row, vec],
                  out_shape=[jax.ShapeDtypeStruct((1, 1), F32), jax.ShapeDtypeStruct((S, D), F32),
                             jax.ShapeDtypeStruct((1, D), F32)],
                  compiler_params=_params("arbitrary"))(x, g, tgt)


def _sigmoid(z):
    return 1.0 / (1.0 + jnp.exp(-z))


def _gate_fwd(gl, bg, ya, yb, *, name, tm=512):
    S, D = ya.shape

    def body(za_ref, zb_ref, ba_ref, bb_ref, ya_ref, yb_ref, o_ref):
        ga = _sigmoid(za_ref[...].astype(F32) + ba_ref[...])
        gb = _sigmoid(zb_ref[...].astype(F32) + bb_ref[...])
        o_ref[...] = (ga * ya_ref[...].astype(F32) + gb * yb_ref[...].astype(F32)).astype(o_ref.dtype)

    lo = pl.BlockSpec((tm, D), lambda i: (i, 0))
    hi = pl.BlockSpec((tm, D), lambda i: (i, 1))
    vlo = pl.BlockSpec((1, D), lambda i: (0, 0))
    vhi = pl.BlockSpec((1, D), lambda i: (0, 1))
    return _pcall(body, name=name, grid=(S // tm,), in_specs=[lo, hi, vlo, vhi, lo, lo], out_specs=lo,
                  out_shape=jax.ShapeDtypeStruct((S, D), _CD), compiler_params=_params("parallel"))(gl, gl, bg, bg, ya, yb)


def _gate_bwd(dm, gl, bg, ya, yb, *, name, tm=256):
    S, D = ya.shape

    def body(dm_ref, za_ref, zb_ref, ba_ref, bb_ref, ya_ref, yb_ref, dya_ref, dyb_ref, dgl_ref, dbg_ref):
        dmv = dm_ref[...].astype(F32)
        ga = _sigmoid(za_ref[...].astype(F32) + ba_ref[...])
        gb = _sigmoid(zb_ref[...].astype(F32) + bb_ref[...])
        dya_ref[...] = (dmv * ga).astype(dya_ref.dtype)
        dyb_ref[...] = (dmv * gb).astype(dyb_ref.dtype)
        dza = dmv * ya_ref[...].astype(F32) * ga * (1.0 - ga)
        dzb = dmv * yb_ref[...].astype(F32) * gb * (1.0 - gb)
        dgl_ref[:, :D] = dza.astype(dgl_ref.dtype)
        dgl_ref[:, D:] = dzb.astype(dgl_ref.dtype)
        pa = jnp.sum(dza, axis=0, keepdims=True)
        pb = jnp.sum(dzb, axis=0, keepdims=True)

        @pl.when(pl.program_id(0) == 0)
        def _():
            dbg_ref[:, :D] = pa
            dbg_ref[:, D:] = pb

        @pl.when(pl.program_id(0) > 0)
        def _():
            dbg_ref[:, :D] += pa
            dbg_ref[:, D:] += pb

    lo = pl.BlockSpec((tm, D), lambda i: (i, 0))
    hi = pl.BlockSpec((tm, D), lambda i: (i, 1))
    vlo = pl.BlockSpec((1, D), lambda i: (0, 0))
    vhi = pl.BlockSpec((1, D), lambda i: (0, 1))
    wide = pl.BlockSpec((tm, 2 * D), lambda i: (i, 0))
    vwide = pl.BlockSpec((1, 2 * D), lambda i: (0, 0))
    return _pcall(body, name=name, grid=(S // tm,), in_specs=[lo, lo, hi, vlo, vhi, lo, lo],
                  out_specs=[lo, lo, wide, vwide],
                  out_shape=[jax.ShapeDtypeStruct((S, D), _CD), jax.ShapeDtypeStruct((S, D), _CD),
                             jax.ShapeDtypeStruct((S, 2 * D), _CD), jax.ShapeDtypeStruct((1, 2 * D), F32)],
                  compiler_params=_params("arbitrary"))(dm, gl, gl, bg, bg, ya, yb)


def _ffn_in_act(h2, w_blocks, *, name, tm=512):
    S, D = h2.shape
    _, _, C = w_blocks.shape

    def body(a_ref, bg_ref, bu_ref, g_ref, u_ref, o_ref):
        av = a_ref[...].astype(_CD)
        gv = jnp.dot(av, bg_ref[0].astype(_CD), preferred_element_type=F32)
        uv = jnp.dot(av, bu_ref[0].astype(_CD), preferred_element_type=F32)
        g_ref[...] = gv.astype(g_ref.dtype)
        u_ref[...] = uv.astype(u_ref.dtype)
        o_ref[...] = (gv * _sigmoid(gv) * uv).astype(o_ref.dtype)

    out = pl.BlockSpec((tm, C), lambda i, j: (i, j))
    shp = jax.ShapeDtypeStruct((S, 2 * C), _CD)
    return _pcall(body, name=name, grid=(S // tm, 2),
                  in_specs=[pl.BlockSpec((tm, D), lambda i, j: (i, 0)), pl.BlockSpec((1, D, C), lambda i, j: (j, 0, 0)),
                            pl.BlockSpec((1, D, C), lambda i, j: (2 + j, 0, 0))],
                  out_specs=[out, out, out], out_shape=[shp, shp, shp],
                  compiler_params=_params("parallel", "arbitrary"))(h2, w_blocks, w_blocks)


def _d_swiglu(dx, w_down, gate, up, *, name, tm=512, tn=1408):
    S, D = dx.shape
    F = w_down.shape[0]
    nt = (((1,), (1,)), ((), ()))

    def body(a_ref, b_ref, g_ref, u_ref, o_ref):
        dv = lax.dot_general(a_ref[...].astype(_CD), b_ref[...].astype(_CD), nt, preferred_element_type=F32)
        gv = g_ref[...].astype(F32)
        sg = _sigmoid(gv)
        o_ref[0] = (dv * u_ref[...].astype(F32) * (sg * (1.0 + gv * (1.0 - sg)))).astype(o_ref.dtype)
        o_ref[1] = (dv * (gv * sg)).astype(o_ref.dtype)

    tile = pl.BlockSpec((tm, tn), lambda i, j: (i, j))
    return _pcall(body, name=name, grid=(S // tm, F // tn),
                  in_specs=[pl.BlockSpec((tm, D), lambda i, j: (i, 0)), pl.BlockSpec((tn, D), lambda i, j: (j, 0)),
                            tile, tile],
                  out_specs=pl.BlockSpec((2, tm, tn), lambda i, j: (0, i, j)),
                  out_shape=jax.ShapeDtypeStruct((2, S, F), _CD),
                  compiler_params=_params("parallel", "arbitrary"))(dx, w_down, gate, up)


def _split3(x):
    hi = x.astype(jnp.bfloat16)
    r1 = x - hi.astype(F32)
    mid = r1.astype(jnp.bfloat16)
    lo = (r1 - mid.astype(F32)).astype(jnp.bfloat16)
    return hi, mid, lo


def _ones_dot_left(ones, x):
    return sum(jnp.dot(ones, p, preferred_element_type=F32) for p in _split3(x))


def _ones_dot_right(x, ones):
    return sum(jnp.dot(p, ones, preferred_element_type=F32) for p in _split3(x))


def _head_sum(x):
    n = x.shape[1]
    r = lax.broadcasted_iota(jnp.int32, (n, n), 0) // HEAD_DIM
    c = lax.broadcasted_iota(jnp.int32, (n, n), 1) // HEAD_DIM
    return _ones_dot_right(x, (r == c).astype(jnp.bfloat16))


def _log_sigmoid(z):
    e = jnp.exp(-jnp.abs(z))
    t = 1.0 + e
    log1p_e = jnp.where(t == 1.0, e, jnp.log(t) * (e / jnp.where(t == 1.0, 1.0, t - 1.0)))
    return jnp.minimum(z, 0.0) - log1p_e


def _fox_cumsum(zf, bf, *, name):
    S, W = zf.shape
    nb = S // 128

    def body(z_ref, b_ref, c_ref):
        tri = (lax.broadcasted_iota(jnp.int32, (128, 128), 0) >= lax.broadcasted_iota(jnp.int32, (128, 128), 1))
        tri = tri.astype(jnp.bfloat16)

        def step(i, carry):
            rows = pl.ds(pl.multiple_of(i * 128, 128), 128)
            lf = _log_sigmoid(z_ref[rows, :] + b_ref[...])
            cb = _ones_dot_left(tri, lf) + carry
            c_ref[rows, :] = cb
            return cb[127:128, :]

        lax.fori_loop(0, nb, step, jnp.zeros((1, W), F32))

    return _pcall(body, name=name, out_shape=jax.ShapeDtypeStruct((S, W), F32),
                  compiler_params=pltpu.CompilerParams(vmem_limit_bytes=VMEM_LIMIT))(zf, bf)


def _fox_cumsum_bwd(dc, zf, bf, *, name):
    S, W = zf.shape
    nb = S // 128

    def body(dc_ref, z_ref, b_ref, dz_ref, db_ref):
        tri = (lax.broadcasted_iota(jnp.int32, (128, 128), 0) <= lax.broadcasted_iota(jnp.int32, (128, 128), 1))
        tri = tri.astype(jnp.bfloat16)

        def step(k, carry):
            tail, acc = carry
            i = nb - 1 - k
            rows = pl.ds(pl.multiple_of(i * 128, 128), 128)
            dlf = _ones_dot_left(tri, dc_ref[rows, :]) + tail
            dz = dlf * _sigmoid(-(z_ref[rows, :] + b_ref[...]))
            dz_ref[rows, :] = dz
            return dlf[0:1, :], acc + jnp.sum(dz, axis=0, keepdims=True)

        _, acc = lax.fori_loop(0, nb, step, (jnp.zeros((1, W), F32), jnp.zeros((1, W), F32)))
        db_ref[...] = acc

    return _pcall(body, name=name,
                  out_shape=[jax.ShapeDtypeStruct((S, W), F32), jax.ShapeDtypeStruct((1, W), F32)],
                  compiler_params=pltpu.CompilerParams(vmem_limit_bytes=VMEM_LIMIT))(dc, zf, bf)


def _proj_dil(h, w_qkv, *, name, tm=1024):
    S, D = h.shape
    tn = DIL_WIDTH

    def body(a_ref, b_ref, *rest):
        outs, acc = rest[:N_DIL_GROUPS], rest[N_DIL_GROUPS]
        prod = jnp.dot(a_ref[...].astype(_CD), b_ref[...].astype(_CD), preferred_element_type=F32)
        for k in range(tn // LANES):
            acc[k] = prod[:, k * LANES:(k + 1) * LANES]
        for g, (_, d) in enumerate(DIL_PAIRS):
            for half in range(DIL_OUT // LANES):
                k = g * (DIL_OUT // LANES) + half
                cols = slice(half * LANES, (half + 1) * LANES)
                for r in range(d):
                    rows = pl.ds(r, tm // d, stride=d) if d > 1 else slice(None)
                    outs[g][0, r, :, cols] = acc[k, rows, :].astype(outs[g].dtype)

    out_specs = [pl.BlockSpec((1, d, tm // d, DIL_OUT), lambda i, j: (j, 0, i, 0)) for _, d in DIL_PAIRS]
    out_shape = [jax.ShapeDtypeStruct((3, d, S // d, DIL_OUT), _CD) for _, d in DIL_PAIRS]
    outs = _pcall(body, name=name, grid=(S // tm, 3),
                  in_specs=[pl.BlockSpec((tm, D), lambda i, j: (i, 0)), pl.BlockSpec((D, tn), lambda i, j: (0, j))],
                  out_specs=out_specs, out_shape=out_shape, scratch_shapes=[pltpu.VMEM((tn // LANES, tm, LANES), F32)],
                  compiler_params=_params("parallel", "arbitrary"))(h, w_qkv)
    return [o.reshape(3, S, DIL_OUT) for o in outs]


def _dil_start(block, S, dilation):
    sub = S // dilation
    u0 = block * DIL_W
    return (u0 % sub) * dilation + u0 // sub


def _dil_slopes(group):
    h = np.arange(1, N_DIL_GROUPS * DIL_HEADS + 1, dtype=np.float32)
    s = (np.float32(2.0) ** (np.float32(-8.0) * h / np.float32(N_DIL_GROUPS * DIL_HEADS))).astype(np.float32)
    return [float(v) for v in s.reshape(N_DIL_GROUPS, DIL_HEADS)[group]]


def _dil_tiles(i, n, blocks_per_seq):
    qi = lax.broadcasted_iota(jnp.int32, (DIL_W, 2 * DIL_W), 0)
    kj = lax.broadcasted_iota(jnp.int32, (DIL_W, 2 * DIL_W), 1)
    rel = qi + DIL_W - kj
    first = ((4 * n + i) % blocks_per_seq) == 0
    valid = jnp.logical_and(jnp.logical_and(rel >= 0, rel <= DIL_W), jnp.logical_or(kj >= DIL_W, jnp.logical_not(first)))
    return valid, rel.astype(F32)


def _dil_window(cur_ref, prev_ref, i, cols):
    if i > 0:
        return cur_ref[(i - 1) * DIL_W:(i + 1) * DIL_W, cols]
    return jnp.concatenate([prev_ref[:, cols], cur_ref[:DIL_W, cols]], axis=0)


CHUNK = 4 * DIL_W


def _dil_rows(block, S, dilation):
    start = _dil_start(block, S, dilation)
    return pl.ds(start, DIL_W, stride=dilation) if dilation > 1 else pl.ds(start, DIL_W)


def SPLIT(S):
    return (DIL_OUT // LANES, S, LANES)


def _dil_fwd(qkv, group, *, name):
    S = qkv.shape[1]
    dilation = DIL_PAIRS[group][1]
    bps = (S // dilation) // DIL_W
    slopes = _dil_slopes(group)
    nt = (((1,), (1,)), ((), ()))

    def body(q_ref, k_ref, v_ref, kp_ref, vp_ref, on_ref, ln_ref, o_ref, l_ref):
        n = pl.program_id(0)
        for i in range(4):
            valid, rel = _dil_tiles(i, n, bps)
            rows = slice(i * DIL_W, (i + 1) * DIL_W)
            for h in range(DIL_HEADS):
                cols = slice(h * HEAD_DIM, (h + 1) * HEAD_DIM)
                qh = q_ref[rows, cols]
                k2, v2 = _dil_window(k_ref, kp_ref, i, cols), _dil_window(v_ref, vp_ref, i, cols)
                s = lax.dot_general(qh, k2, nt, preferred_element_type=F32) * ATTN_SCALE - (slopes[h] * dilation) * rel
                s = jnp.where(valid, s, NEG_INF)
                m = jnp.max(s, axis=-1, keepdims=True)
                p = jnp.exp(s - m)
                den = jnp.sum(p, axis=-1, keepdims=True)
                acc = jnp.dot(p.astype(_CD), v2, preferred_element_type=F32)
                o_ref[rows, cols] = acc / den
                l_ref[rows, cols] = jnp.broadcast_to(m + jnp.log(den), (DIL_W, HEAD_DIM))
        for i in range(4):
            rows = slice(i * DIL_W, (i + 1) * DIL_W)
            nat = _dil_rows(4 * n + i, S, dilation)
            for half in range(DIL_OUT // LANES):
                cols = slice(half * LANES, (half + 1) * LANES)
                on_ref[half, nat, :] = o_ref[rows, cols]
                ln_ref[half, nat, :] = l_ref[rows, cols]

    def cur(which):
        return pl.BlockSpec((None, CHUNK, DIL_OUT), lambda n: (which, n, 0))

    def prev(which):
        return pl.BlockSpec((None, DIL_W, DIL_OUT), lambda n: (which, jnp.maximum(4 * n - 1, 0), 0))

    whole = pl.BlockSpec(SPLIT(S), lambda n: (0, 0, 0))
    return _pcall(body, name=name, grid=(S // CHUNK,), in_specs=[cur(0), cur(1), cur(2), prev(1), prev(2)],
                  out_specs=[whole, whole],
                  out_shape=[jax.ShapeDtypeStruct(SPLIT(S), F32), jax.ShapeDtypeStruct(SPLIT(S), F32)],
                  scratch_shapes=[pltpu.VMEM((CHUNK, DIL_OUT), F32), pltpu.VMEM((CHUNK, DIL_OUT), F32)],
                  compiler_params=_params("arbitrary"))(qkv, qkv, qkv, qkv, qkv)


STAT_OFFSET = HEAD_DIM // 2


def _dil_bwd(qkv, stats, do, group, *, name):
    S = qkv.shape[1]
    dilation = DIL_PAIRS[group][1]
    bps = (S // dilation) // DIL_W
    slopes = _dil_slopes(group)
    nchunk = S // CHUNK
    nt = (((1,), (1,)), ((), ()))
    tn = (((0,), (0,)), ((), ()))

    def body(q_ref, k_ref, v_ref, kp_ref, vp_ref, ln_ref, don_ref, dqn_ref, dkn_ref, dvn_ref,
             dk_s, dv_s, l_ref, do_ref, dq_ref):
        step = pl.program_id(0)
        n = nchunk - 1 - step
        for i in range(4):
            rows = slice(i * DIL_W, (i + 1) * DIL_W)
            nat = _dil_rows(4 * n + i, S, dilation)
            for half in range(DIL_OUT // LANES):
                cols = slice(half * LANES, (half + 1) * LANES)
                l_ref[rows, cols] = ln_ref[half, nat, :]
                do_ref[rows, cols] = don_ref[half, nat, :]

        @pl.when(step == 0)
        def _():
            dk_s[:, CHUNK:] = jnp.zeros((DIL_OUT, DIL_W), F32)
            dv_s[:, CHUNK:] = jnp.zeros((DIL_OUT, DIL_W), F32)

        dk_s[:, :CHUNK] = jnp.zeros((DIL_OUT, CHUNK), F32)
        dv_s[:, :CHUNK] = jnp.zeros((DIL_OUT, CHUNK), F32)
        for i in range(4):
            valid, rel = _dil_tiles(i, n, bps)
            rows = slice(i * DIL_W, (i + 1) * DIL_W)
            window = slice(i * DIL_W, (i + 2) * DIL_W)
            for h in range(DIL_HEADS):
                cols = slice(h * HEAD_DIM, (h + 1) * HEAD_DIM)
                qh = q_ref[rows, cols]
                k2, v2 = _dil_window(k_ref, kp_ref, i, cols), _dil_window(v_ref, vp_ref, i, cols)
                lh = l_ref[rows, h * HEAD_DIM:h * HEAD_DIM + 1]
                shift = l_ref[rows, h * HEAD_DIM + STAT_OFFSET:h * HEAD_DIM + STAT_OFFSET + 1]
                s = lax.dot_general(qh, k2, nt, preferred_element_type=F32) * ATTN_SCALE - (slopes[h] * dilation) * rel
                p = jnp.exp(jnp.where(valid, s, NEG_INF) - lh)
                dob = do_ref[rows, cols].astype(_CD)
                ds = p * (lax.dot_general(dob, v2, nt, preferred_element_type=F32) + shift)
                dsb = (ds * ATTN_SCALE).astype(_CD)
                dq_ref[rows, cols] = jnp.dot(dsb, k2, preferred_element_type=F32)
                dk_s[cols, window] += lax.dot_general(qh, dsb, tn, preferred_element_type=F32)
                dv_s[cols, window] += lax.dot_general(dob, p.astype(_CD), tn, preferred_element_type=F32)
        for i in range(4):
            rows = slice(i * DIL_W, (i + 1) * DIL_W)
            done = slice((i + 1) * DIL_W, (i + 2) * DIL_W)
            nat = _dil_rows(4 * n + i, S, dilation)
            dkb, dvb = dk_s[:, done].T, dv_s[:, done].T
            for half in range(DIL_OUT // LANES):
                cols = slice(half * LANES, (half + 1) * LANES)
                dqn_ref[half, nat, :] = dq_ref[rows, cols]
                dkn_ref[half, nat, :] = dkb[:, cols]
                dvn_ref[half, nat, :] = dvb[:, cols]
        dk_s[:, CHUNK:] = dk_s[:, :DIL_W]
        dv_s[:, CHUNK:] = dv_s[:, :DIL_W]

    def cur(which):
        return pl.BlockSpec((None, CHUNK, DIL_OUT), lambda s: (which, nchunk - 1 - s, 0))

    def prev(which):
        return pl.BlockSpec((None, DIL_W, DIL_OUT), lambda s: (which, jnp.maximum(4 * (nchunk - 1 - s) - 1, 0), 0))

    whole = pl.BlockSpec(SPLIT(S), lambda s: (0, 0, 0))
    shp = jax.ShapeDtypeStruct(SPLIT(S), F32)
    tile = pltpu.VMEM((CHUNK, DIL_OUT), F32)
    return _pcall(body, name=name, grid=(nchunk,),
                  in_specs=[cur(0), cur(1), cur(2), prev(1), prev(2), whole, whole],
                  out_specs=[whole, whole, whole], out_shape=[shp, shp, shp],
                  scratch_shapes=[pltpu.VMEM((DIL_OUT, CHUNK + DIL_W), F32), pltpu.VMEM((DIL_OUT, CHUNK + DIL_W), F32),
                                  tile, tile, tile],
                  compiler_params=pltpu.CompilerParams(dimension_semantics=("arbitrary",),
                                                       vmem_limit_bytes=VMEM_LIMIT_RESIDENT))(
        qkv, qkv, qkv, qkv, qkv, stats, do)


def _dil_mix_fwd(os_, ls_, *, name, tm=512):
    nh, S, _ = os_[0].shape

    def body(o0, o1, o2, l0, l1, l2, out_ref):
        for half in range(nh):
            ls = [l0[half], l1[half], l2[half]]
            m = jnp.maximum(jnp.maximum(ls[0], ls[1]), ls[2])
            es = [jnp.exp(l - m) for l in ls]
            den = es[0] + es[1] + es[2]
            mixed = (es[0] * o0[half] + es[1] * o1[half] + es[2] * o2[half]) / den
            out_ref[:, half * LANES:(half + 1) * LANES] = mixed.astype(out_ref.dtype)

    halves = pl.BlockSpec((nh, tm, LANES), lambda i: (0, i, 0))
    row = pl.BlockSpec((tm, nh * LANES), lambda i: (i, 0))
    return _pcall(body, name=name, grid=(S // tm,), in_specs=[halves] * 6, out_specs=row,
                  out_shape=jax.ShapeDtypeStruct((S, nh * LANES), _CD), compiler_params=_params("parallel"))(*os_, *ls_)


def _dil_mix_bwd(doa, os_, ls_, *, name, tm=512, after=None):
    nh, S, _ = os_[0].shape

    def body(d_ref, o0, o1, o2, l0, l1, l2, do0, do1, do2, st0, st1, st2):
        first = lax.broadcasted_iota(jnp.int32, (tm, LANES), 1) % HEAD_DIM < STAT_OFFSET
        for half in range(nh):
            dv = d_ref[:, half * LANES:(half + 1) * LANES]
            ls = [l0[half], l1[half], l2[half]]
            m = jnp.maximum(jnp.maximum(ls[0], ls[1]), ls[2])
            es = [jnp.exp(l - m) for l in ls]
            den = es[0] + es[1] + es[2]
            al = [e / den for e in es]
            da = [_head_sum(dv * o[half]) for o in (o0, o1, o2)]
            mean = al[0] * da[0] + al[1] * da[1] + al[2] * da[2]
            for a, l, do_ref, st_ref in zip(al, ls, (do0, do1, do2), (st0, st1, st2)):
                do_ref[half] = a * dv
                st_ref[half] = jnp.where(first, l, -a * mean)

    halves = pl.BlockSpec((nh, tm, LANES), lambda i: (0, i, 0))
    row = pl.BlockSpec((tm, nh * LANES), lambda i: (i, 0))
    shp = jax.ShapeDtypeStruct((nh, S, LANES), F32)
    return _pcall(body, after, name=name, grid=(S // tm,), in_specs=[row] + [halves] * 6, out_specs=[halves] * 6,
                  out_shape=[shp] * 6, compiler_params=_params("parallel"))(doa, *os_, *ls_)


FOX_T = 512


PACK = 2 * HEAD_DIM
HEAD_PAIRS = N_FOX_HEADS // 2
FOX_HPS = 8
Q_BLOCK0 = 0
K_BLOCK0 = FOX_WIDTH // PACK
V_BLOCK0 = 2 * FOX_WIDTH // PACK


def _pieces(x):
    hi = x.astype(jnp.bfloat16).astype(F32)
    r = x - hi
    mid = r.astype(jnp.bfloat16).astype(F32)
    lo = (r - mid).astype(jnp.bfloat16).astype(F32)
    return [hi, mid, lo]


def _extras(first, second, rows):
    lane = lax.broadcasted_iota(jnp.int32, (rows, HEAD_DIM), 1)
    out = jnp.zeros((rows, HEAD_DIM), F32)
    for base, triple in ((0, first), (3, second)):
        if all(isinstance(v, float) for v in triple) and len(set(triple)) == 1:
            if triple[0] != 0.0:
                out = jnp.where(jnp.logical_and(lane >= base, lane < base + 3), triple[0], out)
        else:
            for idx, val in enumerate(triple):
                out = jnp.where(lane == base + idx, val, out)
    return out


def _head_column(c, h):
    lane = lax.broadcasted_iota(jnp.int32, c.shape, 1)
    return jnp.sum(jnp.where(lane == h, c, 0.0), axis=1, keepdims=True)


ONES3 = [1.0, 1.0, 1.0]
ZEROS3 = [0.0, 0.0, 0.0]


def _fox_pack_fwd(qkv, c, *, name, tm=512):
    S = qkv.shape[0]

    def body(q_ref, k_ref, v_ref, c_ref, qo_ref, ko_ref, vo_ref):
        hp = pl.program_id(1)
        cv = c_ref[...]
        v_extras = jnp.where(lax.broadcasted_iota(jnp.int32, (tm, HEAD_DIM), 1) < 3, 1.0, 0.0).astype(vo_ref.dtype)
        for hh in range(2):
            ch = _pieces(_head_column(cv, 2 * hp + hh))
            src = slice(hh * HEAD_DIM, (hh + 1) * HEAD_DIM)
            lo = slice(hh * PACK, hh * PACK + HEAD_DIM)
            hi = slice(hh * PACK + HEAD_DIM, (hh + 1) * PACK)
            qo_ref[:, lo] = (q_ref[:, src].astype(F32) * ATTN_SCALE).astype(qo_ref.dtype)
            qo_ref[:, hi] = _extras(ch, ONES3, tm).astype(qo_ref.dtype)
            ko_ref[:, lo] = k_ref[:, src]
            ko_ref[:, hi] = _extras(ONES3, [-p for p in ch], tm).astype(ko_ref.dtype)
            vo_ref[:, lo] = v_ref[:, src]
            vo_ref[:, hi] = v_extras

    def src(block0):
        return pl.BlockSpec((tm, PACK), lambda i, hp: (i, block0 + hp))

    out = pl.BlockSpec((tm, 2 * PACK), lambda i, hp: (i, hp))
    shp = jax.ShapeDtypeStruct((S, N_FOX_HEADS * PACK), _CD)
    return _pcall(body, name=name, grid=(S // tm, HEAD_PAIRS),
                  in_specs=[src(Q_BLOCK0), src(K_BLOCK0), src(V_BLOCK0), pl.BlockSpec((tm, PACK), lambda i, hp: (i, 0))],
                  out_specs=[out, out, out], out_shape=[shp, shp, shp],
                  compiler_params=_params("parallel", "parallel"))(qkv, qkv, qkv, c)


def _fox_fwd(qp, kp, vp, *, name):
    S = qp.shape[0]
    nt = S // FOX_T
    nt_dims = (((1,), (1,)), ((), ()))
    tn_dims = (((0,), (0,)), ((), ()))

    def body(i_tab, j_tab, q_ref, k_ref, v_ref, o_ref, l_ref, m_s, acc_s):
        t = pl.program_id(1)
        i, j = i_tab[t], j_tab[t]

        @pl.when(j == 0)
        def _():
            m_s[...] = jnp.full((FOX_HPS, 1, FOX_T), NEG_INF, F32)
            acc_s[...] = jnp.zeros((FOX_HPS, PACK, FOX_T), F32)

        def part(hh, keys, q0, diagonal):
            cols = slice(hh * PACK, (hh + 1) * PACK)
            nq = FOX_T - q0 if keys == FOX_T else keys
            qr = slice(q0, q0 + nq)
            st = lax.dot_general(k_ref[:keys, cols], q_ref[qr, cols], nt_dims, preferred_element_type=F32)
            if diagonal:
                key = lax.broadcasted_iota(jnp.int32, (keys, nq), 0)
                qry = lax.broadcasted_iota(jnp.int32, (keys, nq), 1) + q0
                st = jnp.where(key <= qry, st, NEG_INF)
            m_old = m_s[hh, :, qr]
            m_new = jnp.maximum(m_old, jnp.max(st, axis=0, keepdims=True))
            pt = jnp.exp(st - m_new)
            acc_s[hh, :, qr] = jnp.exp(m_old - m_new) * acc_s[hh, :, qr] + lax.dot_general(
                v_ref[:keys, cols], pt.astype(_CD), tn_dims, preferred_element_type=F32)
            m_s[hh, :, qr] = m_new

        def tile(diagonal):
            for hh in range(FOX_HPS):
                if diagonal:
                    part(hh, FOX_T // 2, 0, True)
                    part(hh, FOX_T, FOX_T // 2, True)
                else:
                    part(hh, FOX_T, 0, False)

        @pl.when(j < i)
        def _():
            tile(False)

        @pl.when(j == i)
        def _():
            tile(True)
            for hh in range(FOX_HPS):
                acc = acc_s[hh]
                den = acc[HEAD_DIM:HEAD_DIM + 1, :]
                cols = slice(hh * HEAD_DIM, (hh + 1) * HEAD_DIM)
                o_ref[:, cols] = (acc[:HEAD_DIM, :] / den).T
                l_ref[:, cols] = jnp.broadcast_to(m_s[hh] + jnp.log(den), (HEAD_DIM, FOX_T)).T

    pairs = [(i, j) for i in range(nt) for j in range(i + 1)]
    i_tab = jnp.asarray([p[0] for p in pairs], jnp.int32)
    j_tab = jnp.asarray([p[1] for p in pairs], jnp.int32)
    qs = pl.BlockSpec((FOX_T, FOX_HPS * PACK), lambda hp, t, it, jt: (it[t], hp))
    ks = pl.BlockSpec((FOX_T, FOX_HPS * PACK), lambda hp, t, it, jt: (jt[t], hp))
    os_ = pl.BlockSpec((FOX_T, FOX_HPS * HEAD_DIM), lambda hp, t, it, jt: (it[t], hp))
    shp = jax.ShapeDtypeStruct((S, FOX_WIDTH), F32)
    grid_spec = pltpu.PrefetchScalarGridSpec(
        num_scalar_prefetch=2, grid=(N_FOX_HEADS // FOX_HPS, len(pairs)), in_specs=[qs, ks, ks], out_specs=[os_, os_],
        scratch_shapes=[pltpu.VMEM((FOX_HPS, 1, FOX_T), F32), pltpu.VMEM((FOX_HPS, PACK, FOX_T), F32)])
    return _pcall(body, name=name, grid_spec=grid_spec, out_shape=[shp, shp],
                  compiler_params=_params("parallel", "arbitrary"))(i_tab, j_tab, qp, kp, vp)


def _fox_pack_bwd(qkv, c, o, lse, do, *, name, tm=512, after=None):
    S = qkv.shape[0]

    def body(q_ref, c_ref, o_ref, l_ref, do_ref, qo_ref, do_out_ref):
        hp = pl.program_id(1)
        cv = c_ref[...]
        for hh in range(2):
            src = slice(hh * HEAD_DIM, (hh + 1) * HEAD_DIM)
            lo = slice(hh * PACK, hh * PACK + HEAD_DIM)
            hi = slice(hh * PACK + HEAD_DIM, (hh + 1) * PACK)
            shift = _head_column(cv, 2 * hp + hh) - l_ref[:, hh * HEAD_DIM:hh * HEAD_DIM + 1]
            dov = do_ref[:, src]
            dsum = jnp.sum(dov * o_ref[:, src], axis=-1, keepdims=True)
            qo_ref[:, lo] = (q_ref[:, src].astype(F32) * ATTN_SCALE).astype(qo_ref.dtype)
            qo_ref[:, hi] = _extras(_pieces(shift), ONES3, tm).astype(qo_ref.dtype)
            do_out_ref[:, lo] = dov.astype(do_out_ref.dtype)
            do_out_ref[:, hi] = _extras(_pieces(-dsum), ZEROS3, tm).astype(do_out_ref.dtype)

    pair = pl.BlockSpec((tm, PACK), lambda i, hp: (i, hp))
    out = pl.BlockSpec((tm, 2 * PACK), lambda i, hp: (i, hp))
    shp = jax.ShapeDtypeStruct((S, N_FOX_HEADS * PACK), _CD)
    return _pcall(body, after, name=name, grid=(S // tm, HEAD_PAIRS),
                  in_specs=[pl.BlockSpec((tm, PACK), lambda i, hp: (i, Q_BLOCK0 + hp)),
                            pl.BlockSpec((tm, PACK), lambda i, hp: (i, 0)), pair, pair, pair],
                  out_specs=[out, out], out_shape=[shp, shp],
                  compiler_params=_params("parallel", "parallel"))(qkv, c, o, lse, do)


def _fox_bwd(qp, kp, vp, dop, *, name):
    S = qp.shape[0]
    nt = S // FOX_T
    nt_dims = (((1,), (1,)), ((), ()))
    tn_dims = (((0,), (0,)), ((), ()))

    def body(i_tab, j_tab, q_ref, k_ref, v_ref, do_ref, dq_ref, dk_ref, dv_ref, dc_ref, dr_ref,
             dq_s, dk_s, dv_s, dc_s, dr_s):
        t = pl.program_id(1)
        i, j = i_tab[t], j_tab[t]

        @pl.when(t == 0)
        def _():
            dq_s[...] = jnp.zeros((S, FOX_HPS * PACK), F32)
            dr_s[...] = jnp.zeros((FOX_HPS, 1, S), F32)

        @pl.when(i == j)
        def _():
            dk_s[...] = jnp.zeros((FOX_T, FOX_HPS * PACK), F32)
            dv_s[...] = jnp.zeros((FOX_T, FOX_HPS * PACK), F32)
            dc_s[...] = jnp.zeros((FOX_HPS, FOX_T, 1), F32)

        def part(hh, keys, q0, diagonal):
            cols = slice(hh * PACK, (hh + 1) * PACK)
            nq = FOX_T - q0 if keys == FOX_T else keys
            kr, qr = slice(0, keys), slice(q0, q0 + nq)
            rows = pl.ds(pl.multiple_of(i * FOX_T + q0, FOX_T // 2), nq)
            qv, kv, vv, dov = q_ref[qr, cols], k_ref[kr, cols], v_ref[kr, cols], do_ref[qr, cols]
            pt = jnp.exp(lax.dot_general(kv, qv, nt_dims, preferred_element_type=F32))
            if diagonal:
                key = lax.broadcasted_iota(jnp.int32, (keys, nq), 0)
                qry = lax.broadcasted_iota(jnp.int32, (keys, nq), 1) + q0
                pt = jnp.where(key <= qry, pt, 0.0)
            dst = pt * lax.dot_general(vv, dov, nt_dims, preferred_element_type=F32)
            dsb = dst.astype(_CD)
            dc_s[hh, kr] += jnp.sum(dst, axis=1, keepdims=True)
            dr_s[hh, :, rows] += jnp.sum(dst, axis=0, keepdims=True)
            dv_s[kr, cols] += jnp.dot(pt.astype(_CD), dov, preferred_element_type=F32)
            dk_s[kr, cols] += jnp.dot(dsb, qv, preferred_element_type=F32)
            dq_s[rows, cols] += lax.dot_general(dsb, kv, tn_dims, preferred_element_type=F32)

        def tile(diagonal):
            for hh in range(FOX_HPS):
                if diagonal:
                    part(hh, FOX_T // 2, 0, True)
                    part(hh, FOX_T, FOX_T // 2, True)
                else:
                    part(hh, FOX_T, 0, False)

        @pl.when(i > j)
        def _():
            tile(False)

        @pl.when(i == j)
        def _():
            tile(True)

        @pl.when(i == nt - 1)
        def _():
            for hh in range(FOX_HPS):
                src = slice(hh * PACK, hh * PACK + HEAD_DIM)
                dst_cols = slice(hh * HEAD_DIM, (hh + 1) * HEAD_DIM)
                dk_ref[:, dst_cols] = dk_s[:, src].astype(dk_ref.dtype)
                dv_ref[:, dst_cols] = dv_s[:, src].astype(dv_ref.dtype)
                dc_ref[:, dst_cols] = jnp.broadcast_to(dc_s[hh], (FOX_T, HEAD_DIM))

        @pl.when(t == len(pairs) - 1)
        def _():
            for hh in range(FOX_HPS):
                dq_ref[:, hh * HEAD_DIM:(hh + 1) * HEAD_DIM] = (
                    dq_s[:, hh * PACK:hh * PACK + HEAD_DIM] * ATTN_SCALE).astype(dq_ref.dtype)
            dr_ref[...] = dr_s[...]

    pairs = [(i, j) for j in range(nt) for i in range(j, nt)]
    i_tab = jnp.asarray([p[0] for p in pairs], jnp.int32)
    j_tab = jnp.asarray([p[1] for p in pairs], jnp.int32)
    wide, narrow = FOX_HPS * PACK, FOX_HPS * HEAD_DIM
    qs = pl.BlockSpec((FOX_T, wide), lambda hp, t, it, jt: (it[t], hp))
    ks = pl.BlockSpec((FOX_T, wide), lambda hp, t, it, jt: (jt[t], hp))
    whole = pl.BlockSpec((S, narrow), lambda hp, t, it, jt: (0, hp))
    cs = pl.BlockSpec((FOX_T, narrow), lambda hp, t, it, jt: (jt[t], hp))
    rs = pl.BlockSpec((FOX_HPS, 1, S), lambda hp, t, it, jt: (hp, 0, 0))
    shp = jax.ShapeDtypeStruct((S, FOX_WIDTH), _CD)
    grid_spec = pltpu.PrefetchScalarGridSpec(
        num_scalar_prefetch=2, grid=(N_FOX_HEADS // FOX_HPS, len(pairs)), in_specs=[qs, ks, ks, qs],
        out_specs=[whole, cs, cs, cs, rs],
        scratch_shapes=[pltpu.VMEM((S, wide), F32), pltpu.VMEM((FOX_T, wide), F32),
                        pltpu.VMEM((FOX_T, wide), F32), pltpu.VMEM((FOX_HPS, FOX_T, 1), F32),
                        pltpu.VMEM((FOX_HPS, 1, S), F32)])
    return _pcall(body, name=name, grid_spec=grid_spec,
                  out_shape=[shp, shp, shp, jax.ShapeDtypeStruct((S, FOX_WIDTH), F32),
                             jax.ShapeDtypeStruct((N_FOX_HEADS, 1, S), F32)],
                  compiler_params=_params("parallel", "arbitrary"))(i_tab, j_tab, qp, kp, vp, dop)


def _layer_step(x, tgt, w, p, late_weights=None, grad_sink=None, after=None, first_weights=None):
    S = x.shape[0]
    after_norm, after_proj = after if after is not None else (None, None)
    h = _rms_fwd(x, p["norm_mix_g"], name="rms_mix", after=after_norm)
    if first_weights is not None:
        w = {**w, **first_weights(h)}
    qkv = _mm(h, w["qkv"][:, 3 * DIL_WIDTH:], name="proj_fox", out_dtype=_CD, tn=768, tm=2048, after=after_proj)
    dil_qkv = _proj_dil(h, w["qkv"], name="proj_dil")
    zf = _mm(h, w["f"], name="proj_f")
    gl = _mm(h, w["g"], name="proj_gate", tn=1024, out_dtype=_CD)

    dil_o, dil_l = [], []
    for g in range(N_DIL_GROUPS):
        og, lg = _dil_fwd(dil_qkv[g], g, name=f"dil_fwd{g}")
        dil_o.append(og), dil_l.append(lg)
    o_a = _dil_mix_fwd(dil_o, dil_l, name="dil_mix")

    c = _fox_cumsum(zf, p["b_fgt"], name="fox_cumsum")
    fqp, fkp, fvp = _fox_pack_fwd(qkv, c, name="fox_pack")
    o_b, flse = _fox_fwd(fqp, fkp, fvp, name="fox_fwd")

    if late_weights is not None:
        w = {**w, **late_weights(o_b)}
    y_a = _mm(o_a, w["dil_out"], name="y_a", tn=1024, out_dtype=_CD)
    y_b = _mm(o_b, w["fox_out"], name="y_b", tn=1024, out_dtype=_CD)
    merged = _gate_fwd(gl, p["b_gate"], y_a, y_b, name="gate_fwd")
    x1 = _mm(merged, w["out"], name="mix_out", add=x)

    h2 = _rms_fwd(x1, p["norm_ffn_g"], name="rms_ffn")
    gate, up, act = _ffn_in_act(h2, w["ffn_in"], name="ffn_in")
    x2 = _mm(act, w["ffn_down"], name="ffn_down", add=x1, tk=2816)

    loss, dx2, dg_final = _loss_head(x2, p["norm_final_g"], tgt, name="loss_head")

    gw_ffn_down = _mm(act, dx2, name="gw_ffn_down", ta=True, out_dtype=_CD, tm=1408)
    dgu = _d_swiglu(dx2, w["ffn_down"], gate, up, name="d_swiglu")
    dh2 = _mm(dgu, w["ffn_in"], name="d_h2", tb=True, tk=1408, b_blocks=True, tm=2048, a_halves=True)
    gw_ffn_in = _mm(h2, dgu, name="gw_ffn_in", ta=True, out_dtype=_CD, tn=1408, out_blocks=1408, b_halves=True)
    sink = grad_sink if grad_sink is not None else (lambda group, grads: None)
    tok = sink("ffn", dict(ffn_in=gw_ffn_in, ffn_down=gw_ffn_down))
    dx1, dg_ffn = _rms_bwd(x1, p["norm_ffn_g"], dh2, dx2, name="rms_ffn_bwd", after=tok)

    dmerged = _mm(dx1, w["out"], name="d_merged", tb=True, out_dtype=_CD)
    gw_out = _mm(merged, dx1, name="gw_out", ta=True, out_dtype=_CD)
    dy_a, dy_b, dgl, db_gate = _gate_bwd(dmerged, gl, p["b_gate"], y_a, y_b, name="gate_bwd")
    do_a = _mm(dy_a, w["dil_out"], name="d_o_a", tb=True)
    gw_dil_out = _mm(o_a, dy_a, name="gw_dil_out", ta=True, out_dtype=_CD, tn=1024)
    do_b = _mm(dy_b, w["fox_out"], name="d_o_b", tb=True)
    gw_fox_out = _mm(o_b, dy_b, name="gw_fox_out", ta=True, out_dtype=_CD, tn=1024)
    tok = sink("mix", dict(dil_out=gw_dil_out, fox_out=gw_fox_out, out=gw_out))

    bqp, bdop = _fox_pack_bwd(qkv, c, o_b, flse, do_b, name="fox_pack_bwd", after=tok)
    dqp, dkp, dvp, dck, dcq = _fox_bwd(bqp, fkp, fvp, bdop, name="fox_bwd")
    dc = dcq[:, 0, :].T - dck.reshape(S, N_FOX_HEADS, HEAD_DIM)[:, :, 0]
    dc = jnp.pad(dc, ((0, 0), (0, F_PAD - N_FOX_HEADS)))
    dzf, db_fgt = _fox_cumsum_bwd(dc, zf, p["b_fgt"], name="fox_cumsum_bwd")

    douts = _dil_mix_bwd(do_a, dil_o, dil_l, name="dil_mix_bwd", after=tok)
    dqs, dks, dvs = [], [], []
    for g in range(N_DIL_GROUPS):
        dq, dk, dv = _dil_bwd(dil_qkv[g], douts[3 + g], douts[g], g, name=f"dil_bwd{g}")
        for parts, t in ((dqs, dq), (dks, dk), (dvs, dv)):
            parts.extend([t[0].astype(_CD), t[1].astype(_CD)])
    dqkv = jnp.concatenate(dqs + dks + dvs + [dqp, dkp, dvp], axis=1)

    gw_qkv = _mm(h, dqkv, name="gw_qkv", ta=True, out_dtype=_CD, tn=768)
    gw_g = _mm(h, dgl, name="gw_gate", ta=True, out_dtype=_CD)
    gw_f = _mm(h, dzf, name="gw_f", ta=True, out_dtype=_CD)
    tok = sink("in", dict(qkv=gw_qkv, f=gw_f, g=gw_g))
    dh = _mm(dqkv, w["qkv"], name="d_h_qkv", tb=True, tk=1920, tm=2048, after=tok)
    dh = _mm(dgl, w["g"], name="d_h_gate", tb=True, add=dh)
    dh = _mm(dzf, w["f"], name="d_h_f", tb=True, add=dh)
    dx, dg_mix = _rms_bwd(x, p["norm_mix_g"], dh, dx1, name="rms_mix_bwd")

    gw = dict(qkv=gw_qkv, f=gw_f, g=gw_g, dil_out=gw_dil_out, fox_out=gw_fox_out, out=gw_out, ffn_in=gw_ffn_in,
              ffn_down=gw_ffn_down)
    small = dict(norm_mix_g=dg_mix, b_fgt=db_fgt, b_gate=db_gate, norm_ffn_g=dg_ffn, norm_final_g=dg_final)
    return loss, dx, gw, small


def _position():
    return lax.axis_index("x"), lax.axis_index("y"), lax.axis_index("c")


def _other_chips(x, y):
    return [(1 - x, y), (x, 1 - y), (1 - x, 1 - y)]


ROW_TILE = 16


def _row_chunks(rows, want=4):
    n = want
    while n > 1 and rows % (n * ROW_TILE):
        n //= 2
    return n


SEM_SPEC = pl.BlockSpec(memory_space=pltpu.SEMAPHORE)
ANY_SPEC = pl.BlockSpec(memory_space=pl.ANY)
DATAFLOW = pltpu.SideEffectType.DATAFLOW_SIDE_EFFECTING


def _in_hbm(a):
    return pltpu.with_memory_space_constraint(a, pltpu.HBM)


def _split_copy_start(srcs, land_shapes, copies, after, *, name):
    n, m = len(srcs), len(land_shapes)

    def body(*refs):
        src_refs, land_refs = refs[:n], refs[n:n + m]
        send_sems, recv_sems = refs[n + m + 1], refs[n + m + 2]
        token = refs[-1]
        x, y, c = _position()
        for k, (src, dst, peer) in enumerate(copies(x, y, c, src_refs, land_refs)):
            pltpu.make_async_remote_copy(src_ref=src, dst_ref=dst, send_sem=send_sems.at[k], recv_sem=recv_sems.at[k],
                                         device_id=peer, device_id_type=MESH).start()
        token[...] = jnp.zeros_like(token)

    lands = [lax.empty(s.shape, s.dtype) for s in land_shapes]
    count = len(copies(0, 0, 0, srcs, lands))
    out = _pcall(
        body, name=name,
        out_shape=(pltpu.SemaphoreType.DMA((count,)), pltpu.SemaphoreType.DMA((count,)),
                   *[pltpu.HBM(s.shape, s.dtype) for s in srcs], *[pltpu.HBM(s.shape, s.dtype) for s in land_shapes],
                   jax.ShapeDtypeStruct((8, 128), F32)),
        in_specs=[HBM_SPEC] * (n + m) + [ANY_SPEC],
        out_specs=(SEM_SPEC, SEM_SPEC, *[HBM_SPEC] * (n + m), pl.BlockSpec(memory_space=pltpu.VMEM)),
        input_output_aliases={k: 2 + k for k in range(n + m)},
        compiler_params=pltpu.CompilerParams(has_side_effects=DATAFLOW),
    )(*[_in_hbm(s) for s in srcs], *[_in_hbm(l) for l in lands], after)
    return out[0], out[1], list(out[2:2 + n]), list(out[2 + n:2 + n + m]), out[-1]


def _split_copy_wait(send_sems, recv_sems, srcs, lands, copies, after, *, name):
    n, m = len(srcs), len(lands)

    def body(*refs):
        src_refs, land_refs = refs[:n], refs[n:n + m]
        send, recv = refs[n + m], refs[n + m + 1]
        x, y, c = _position()
        for k, (src, dst, peer) in enumerate(copies(x, y, c, src_refs, land_refs)):
            cp = pltpu.make_async_remote_copy(src_ref=src, dst_ref=dst, send_sem=send.at[k], recv_sem=recv.at[k],
                                              device_id=peer, device_id_type=MESH)
            cp.wait_send()
            cp.wait_recv()

    afters = list(after) if isinstance(after, (list, tuple)) else [after]
    out = _pcall(
        body, name=name,
        out_shape=tuple(pltpu.HBM(s.shape, s.dtype) for s in list(srcs) + list(lands)),
        in_specs=[HBM_SPEC] * (n + m) + [SEM_SPEC, SEM_SPEC] + [ANY_SPEC] * len(afters),
        out_specs=tuple([HBM_SPEC] * (n + m)),
        input_output_aliases={k: k for k in range(n + m)},
        compiler_params=pltpu.CompilerParams(has_side_effects=DATAFLOW),
    )(*srcs, *lands, send_sems, recv_sems, *afters)
    return list(out[:n]), list(out[n:])


def _gather_copies(x, y, c, shard_refs, land_refs):
    out = []
    for s, l in zip(shard_refs, land_refs):
        half = s.shape[0] // 2
        nq = _row_chunks(half)
        for cx, cy in _other_chips(x, y):
            for q in range(nq):
                rows = pl.ds(c * half + q * (half // nq), half // nq)
                out.append((s.at[rows, :], l.at[2 * x + y, rows, :], (cx, cy, c)))
    return out


def _gather_whole_copies(x, y, c, shard_refs, land_refs):
    out = []
    for s, l in zip(shard_refs, land_refs):
        nq = _row_chunks(s.shape[0])
        for cx, cy in _other_chips(x, y):
            for q in range(nq):
                rows = pl.ds(q * (s.shape[0] // nq), s.shape[0] // nq)
                out.append((s.at[rows, :], l.at[2 * x + y, rows, :], (cx, cy, c)))
    return out


def _scatter_copies(x, y, c, part_refs, land_refs):
    out = []
    for p, l in zip(part_refs, land_refs):
        nq = _row_chunks(p.shape[1])
        for r, (cx, cy) in enumerate(_other_chips(x, y)):
            for q in range(nq):
                rows = pl.ds(q * (p.shape[1] // nq), p.shape[1] // nq)
                out.append((p.at[2 * cx + cy, rows, :], l.at[r, rows, :], (cx, cy, c)))
    return out


def _forward_halves(lands, *, name):
    n = len(lands)

    def body(*refs):
        ins = refs[:n]
        send_sems, recv_sems = refs[2 * n:]
        x, y, c = _position()
        copies = []
        for w in range(n):
            half = ins[w].shape[1] // 2
            for r, (cx, cy) in enumerate(_other_chips(x, y)):
                blk = ins[w].at[2 * cx + cy, pl.ds(c * half, half), :]
                cp = pltpu.make_async_remote_copy(src_ref=blk, dst_ref=blk, send_sem=send_sems.at[w, r],
                                                  recv_sem=recv_sems.at[w, r], device_id=(x, y, 1 - c),
                                                  device_id_type=MESH)
                cp.start()
                copies.append(cp)
        for w in range(n):
            half = ins[w].shape[1] // 2
            for r, (cx, cy) in enumerate(_other_chips(x, y)):
                blk = ins[w].at[2 * cx + cy, pl.ds((1 - c) * half, half), :]
                pltpu.make_async_remote_copy(src_ref=blk, dst_ref=blk, send_sem=send_sems.at[w, r],
                                             recv_sem=recv_sems.at[w, r], device_id=(x, y, 1 - c),
                                             device_id_type=MESH).wait_recv()
        for cp in copies:
            cp.wait_send()

    return _pcall(
        body, name=name, in_specs=[HBM_SPEC] * n, out_specs=[HBM_SPEC] * n,
        out_shape=[jax.ShapeDtypeStruct(l.shape, l.dtype) for l in lands],
        input_output_aliases={k: k for k in range(n)},
        scratch_shapes=[pltpu.SemaphoreType.DMA((n, 3)), pltpu.SemaphoreType.DMA((n, 3))],
    )(*lands)


def _swap_halves(grads, name="swap_halves"):
    n = len(grads)

    def body(*refs):
        ins, outs = refs[:n], refs[n:2 * n]
        send_sems, recv_sems = refs[2 * n:]
        x, y, c = _position()
        copies = []
        for w in range(n):
            half = ins[w].shape[1] // 2
            cp = pltpu.make_async_remote_copy(
                src_ref=ins[w].at[:, pl.ds((1 - c) * half, half), :], dst_ref=outs[w], send_sem=send_sems.at[w],
                recv_sem=recv_sems.at[w], device_id=(x, y, 1 - c), device_id_type=MESH)
            cp.start()
            copies.append(cp)
        for cp in copies:
            cp.wait()

    return _pcall(
        body, name=name, in_specs=[HBM_SPEC] * n, out_specs=[HBM_SPEC] * n,
        out_shape=[jax.ShapeDtypeStruct((4, g.shape[1] // 2, g.shape[2]), g.dtype) for g in grads],
        scratch_shapes=[pltpu.SemaphoreType.DMA((n,)), pltpu.SemaphoreType.DMA((n,))],
    )(*grads)


def _share_halves(halves):
    n = len(halves)

    def body(*refs):
        ins, outs = refs[:n], refs[n:2 * n]
        send_sems, recv_sems = refs[2 * n:]
        x, y, c = _position()
        copies = []
        for w in range(n):
            cp = pltpu.make_async_remote_copy(src_ref=ins[w], dst_ref=outs[w], send_sem=send_sems.at[w],
                                              recv_sem=recv_sems.at[w], device_id=(x, y, 1 - c), device_id_type=MESH)
            cp.start()
            copies.append(cp)
        for cp in copies:
            cp.wait()

    return _pcall(
        body, name="share_halves", in_specs=[HBM_SPEC] * n, out_specs=[HBM_SPEC] * n,
        out_shape=[jax.ShapeDtypeStruct(h.shape, h.dtype) for h in halves],
        scratch_shapes=[pltpu.SemaphoreType.DMA((n,)), pltpu.SemaphoreType.DMA((n,))],
    )(*halves)


def _sum_small(part):
    rows, width = part.shape

    def body(x_ref, out_ref, all_ref, send_sems, recv_sems):
        x, y, c = _position()
        me, sibling = (x, y, c), (x, y, 1 - c)
        chips = _other_chips(x, y)

        def block(px, py, pc):
            return all_ref.at[pl.ds((4 * px + 2 * py + pc) * rows, rows), :]

        def copy(k, blk, to, src=None):
            return pltpu.make_async_remote_copy(
                src_ref=block(*blk) if src is None else src, dst_ref=block(*blk), send_sem=send_sems.at[k],
                recv_sem=recv_sems.at[k], device_id=to, device_id_type=MESH)

        all_ref[pl.ds((4 * x + 2 * y + c) * rows, rows), :] = x_ref[...]
        first = [copy(0, me, sibling, src=x_ref)]
        first += [copy(1 + j, me, (*chip, c), src=x_ref) for j, chip in enumerate(chips)]
        for cp in first:
            cp.start()
        passed = [copy(4 + j, (*chip, c), sibling) for j, chip in enumerate(chips)]
        for j, chip in enumerate(chips):
            copy(1 + j, (*chip, c), me).wait_recv()
            passed[j].start()
        copy(0, sibling, me).wait_recv()
        for j, chip in enumerate(chips):
            copy(4 + j, (*chip, 1 - c), me).wait_recv()
        for cp in first + passed:
            cp.wait_send()
        total = all_ref[0:rows, :]
        for d in range(1, 8):
            total = total + all_ref[d * rows:(d + 1) * rows, :]
        out_ref[...] = total

    vm = pl.BlockSpec(memory_space=pltpu.VMEM)
    return _pcall(
        body, name="sum_small", in_specs=[vm], out_specs=vm, out_shape=jax.ShapeDtypeStruct((rows, width), F32),
        scratch_shapes=[pltpu.VMEM((8 * rows, width), F32), pltpu.SemaphoreType.DMA((7,)), pltpu.SemaphoreType.DMA((7,))],
    )(part)


def _row_tile(R, C, itemsize=4, budget=1 << 20):
    for t in (512, 256, 128, 64, 32, 16, 8):
        if R % t == 0 and t * C * itemsize <= budget:
            return t
    return R


def _add_halves(g, recv, c, *, name):
    _, R, C = g.shape
    half = R // 2
    t = _row_tile(half, C)
    nb = half // t

    def body(c_ref, g_ref, r_ref, o_ref):
        o_ref[...] = (g_ref[...].astype(F32) + r_ref[...].astype(F32)).astype(o_ref.dtype)

    grid_spec = pltpu.PrefetchScalarGridSpec(
        num_scalar_prefetch=1, grid=(4, nb),
        in_specs=[pl.BlockSpec((1, t, C), lambda k, i, cr: (k, cr[0] * nb + i, 0)),
                  pl.BlockSpec((1, t, C), lambda k, i, cr: (k, i, 0))],
        out_specs=pl.BlockSpec((1, t, C), lambda k, i, cr: (k, i, 0)))
    return _pcall(body, name=name, grid_spec=grid_spec, out_shape=jax.ShapeDtypeStruct((4, half, C), g.dtype),
                  compiler_params=_params("parallel", "parallel"))(c, g, recv)


def _add_owners(mine, recv, *, name):
    half, C = mine.shape
    t = _row_tile(half, C)

    def body(m_ref, r_ref, o_ref):
        o_ref[...] = ((m_ref[...].astype(F32) + r_ref[0].astype(F32)) + r_ref[1].astype(F32)) + r_ref[2].astype(F32)

    return _pcall(body, name=name, grid=(half // t,),
                  in_specs=[pl.BlockSpec((t, C), lambda i: (i, 0)), pl.BlockSpec((3, t, C), lambda i: (0, i, 0))],
                  out_specs=pl.BlockSpec((t, C), lambda i: (i, 0)), out_shape=jax.ShapeDtypeStruct((half, C), F32),
                  compiler_params=_params("parallel"))(mine, recv)


def _adamw(w, g, m, v, *, name):
    R, C = w.shape
    t = _row_tile(R, C)
    c1 = 1.0 - ADAM_B1 ** ADAM_STEP
    c2 = 1.0 - ADAM_B2 ** ADAM_STEP

    def body(w_ref, g_ref, m_ref, v_ref, d_ref, nm_ref, nv_ref):
        gv = g_ref[...]
        mn = ADAM_B1 * m_ref[...] + (1.0 - ADAM_B1) * gv
        vn = ADAM_B2 * v_ref[...] + (1.0 - ADAM_B2) * (gv * gv)
        d_ref[...] = -ADAM_LR * ((mn / c1) / (jnp.sqrt(vn / c2) + ADAM_EPS) + ADAM_WD * w_ref[...])
        nm_ref[...] = mn
        nv_ref[...] = vn

    blk = pl.BlockSpec((t, C), lambda i: (i, 0))
    shp = jax.ShapeDtypeStruct((R, C), F32)
    return _pcall(body, name=name, grid=(R // t,), in_specs=[blk] * 4, out_specs=[blk] * 3, out_shape=[shp] * 3,
                  compiler_params=_params("parallel"))(w, g, m, v)


BIG = ("w_in", "w_dil_out", "w_fox_out", "w_out", "w_ffn_in", "w_ffn_down")
SMALL = ("norm_mix_g", "b_fgt", "b_gate", "norm_ffn_g", "norm_final_g")
ORDER = ("norm_mix_g", "w_in", "b_fgt", "b_gate", "w_dil_out", "w_fox_out", "w_out", "norm_ffn_g", "w_ffn_in",
         "w_ffn_down", "norm_final_g")
SMALL_ROWS = {"norm_mix_g": (0, 1), "b_gate": (1, 3), "norm_ffn_g": (3, 4), "norm_final_g": (4, 5), "b_fgt": (5, 6)}


def _columns_to_blocks(full, ncol):
    K = full.shape[0]
    return full.reshape(K, 4, ncol).transpose(1, 0, 2)


def _blocks_to_columns(blocks):
    n, K, ncol = blocks.shape
    return blocks.transpose(1, 0, 2).reshape(K, n * ncol)


def kernel(x, norm_mix_g, w_in, b_fgt, b_gate, w_dil_out, w_fox_out, w_out, norm_ffn_g, w_ffn_in, w_ffn_down, norm_final_g, loss_target, m_norm_mix_g, m_w_in, m_b_fgt, m_b_gate, m_w_dil_out, m_w_fox_out, m_w_out, m_norm_ffn_g, m_w_ffn_in, m_w_ffn_down, m_norm_final_g, v_norm_mix_g, v_w_in, v_b_fgt, v_b_gate, v_w_dil_out, v_w_fox_out, v_w_out, v_norm_ffn_g, v_w_ffn_in, v_w_ffn_down, v_norm_final_g):
    weights = dict(norm_mix_g=norm_mix_g, w_in=w_in, b_fgt=b_fgt, b_gate=b_gate, w_dil_out=w_dil_out,
                   w_fox_out=w_fox_out, w_out=w_out, norm_ffn_g=norm_ffn_g, w_ffn_in=w_ffn_in, w_ffn_down=w_ffn_down,
                   norm_final_g=norm_final_g)
    m_in = dict(norm_mix_g=m_norm_mix_g, w_in=m_w_in, b_fgt=m_b_fgt, b_gate=m_b_gate, w_dil_out=m_w_dil_out,
                w_fox_out=m_w_fox_out, w_out=m_w_out, norm_ffn_g=m_norm_ffn_g, w_ffn_in=m_w_ffn_in,
                w_ffn_down=m_w_ffn_down, norm_final_g=m_norm_final_g)
    v_in = dict(norm_mix_g=v_norm_mix_g, w_in=v_w_in, b_fgt=v_b_fgt, b_gate=v_b_gate, w_dil_out=v_w_dil_out,
                w_fox_out=v_w_fox_out, w_out=v_w_out, norm_ffn_g=v_norm_ffn_g, w_ffn_in=v_w_ffn_in,
                w_ffn_down=v_w_ffn_down, norm_final_g=v_norm_final_g)
    c = lax.axis_index("c")
    chip = 2 * lax.axis_index("x") + lax.axis_index("y")

    shards = {n: weights[n][0].astype(_CD) for n in BIG}
    in_shape = jax.ShapeDtypeStruct((4,) + shards["w_in"].shape, _CD)
    send_i, recv_i, in_src, in_land, token_in = _split_copy_start(
        [shards["w_in"]], [in_shape], _gather_copies, norm_mix_g, name="gather_in_start")
    late = BIG[1:]
    send_g, recv_g, late_src, late_land, token = _split_copy_start(
        [shards[n] for n in late], [jax.ShapeDtypeStruct((4,) + shards[n].shape, _CD) for n in late],
        _gather_whole_copies, token_in, name="gather_late_start")
    adam_in = [t[0] + token_in[0, 0] for t in (w_in, m_w_in, v_w_in)]
    p = dict(norm_mix_g=norm_mix_g, b_fgt=jnp.pad(b_fgt, ((0, 0), (0, F_PAD - N_FOX_HEADS))), b_gate=b_gate,
             norm_ffn_g=norm_ffn_g, norm_final_g=norm_final_g.reshape(1, D_MODEL))

    def first_weights(after):
        own, lands = _split_copy_wait(send_i, recv_i, in_src, in_land, _gather_copies, [after] + adam_in,
                                      name="gather_in_wait")
        (g_in,) = _forward_halves(lands, name="gather_in_forward")
        full_in = _blocks_to_columns(lax.dynamic_update_index_in_dim(g_in, own[0], chip, 0))
        o3 = QKV_COLS
        o4 = o3 + N_FOX_HEADS
        return dict(qkv=full_in[:, :o3], f=jnp.pad(full_in[:, o3:o4], ((0, 0), (0, F_PAD - N_FOX_HEADS))),
                    g=full_in[:, o4:])

    def late_weights(after):
        own, lands = _split_copy_wait(send_g, recv_g, late_src, late_land, _gather_whole_copies, after,
                                      name="gather_late_wait")
        g_dil, g_fox, g_out, g_ffn_in, g_ffn_down = [
            lax.dynamic_update_index_in_dim(l, s, chip, 0) for l, s in zip(lands, own)]
        return dict(dil_out=_blocks_to_columns(g_dil), fox_out=_blocks_to_columns(g_fox),
                    out=g_out.reshape(D_MODEL, D_MODEL), ffn_in=g_ffn_in,
                    ffn_down=g_ffn_down.reshape(D_FF, D_MODEL))

    c_arr = jnp.reshape(c, (1,)).astype(jnp.int32)

    def to_blocks(n, full):
        shape = weights[n].shape
        if full.ndim == 3:
            return full
        if n in ("w_out", "w_ffn_down"):
            return full.reshape(4, shape[1], shape[2])
        return _columns_to_blocks(full, shape[2])

    def pair_sums(group, named):
        names = list(named)
        blocks = [to_blocks(n, named[n]) for n in names]
        from_sibling = _swap_halves(blocks, name=f"swap_halves_{group}")
        return [_add_halves(b, r, c_arr, name=f"add_halves_{n}") for b, r, n in zip(blocks, from_sibling, names)]

    in_flight = {}

    def grad_sink(group, gw):
        if group == "in":
            named = {"w_in": jnp.concatenate([gw["qkv"], gw["f"][:, :N_FOX_HEADS], gw["g"]], axis=1)}
        else:
            named = {"w_" + k: v for k, v in gw.items()}
        sums = pair_sums(group, named)
        started = _split_copy_start(sums, [jax.ShapeDtypeStruct((3,) + s.shape[1:], s.dtype) for s in sums],
                                    _scatter_copies, next(iter(gw.values())), name=f"scatter_{group}_start")
        in_flight[group] = (list(named), started)
        return started[-1]

    loss_part, grad_x, gw, small = _layer_step(x[0], loss_target[0], {}, p, late_weights, grad_sink,
                                               (token_in, token), first_weights)

    def owner_sums(names, sums, from_chips):
        return {n: _add_owners(lax.dynamic_index_in_dim(s, chip, 0, keepdims=False), r, name=f"add_owners_{n}")
                for n, s, r in zip(names, sums, from_chips)}

    halves = {}
    for group, (names, (send_s, recv_s, srcs, lands, _)) in in_flight.items():
        sums, from_chips = _split_copy_wait(send_s, recv_s, srcs, lands, _scatter_copies, grad_x,
                                            name=f"scatter_{group}_wait")
        halves.update(owner_sums(names, sums, from_chips))
    halves = [halves[n] for n in BIG]
    grads = {}
    for n, own, other in zip(BIG, halves, _share_halves(halves)):
        pair = jnp.stack([own, other])
        grads[n] = jnp.where(c == 0, pair, pair[::-1]).reshape(2 * own.shape[0], own.shape[1])

    packed = jnp.concatenate([
        small["norm_mix_g"], small["b_gate"].reshape(2, D_MODEL), small["norm_ffn_g"], small["norm_final_g"],
        jnp.pad(small["b_fgt"], ((0, 0), (0, D_MODEL - F_PAD))), jnp.pad(loss_part, ((0, 0), (0, D_MODEL - 1))),
        jnp.zeros((1, D_MODEL), F32)], axis=0)
    summed = _sum_small(packed)
    for n in SMALL:
        lo, hi = SMALL_ROWS[n]
        grads[n] = summed[lo:hi].reshape(1, -1)[:, :weights[n].size]
    loss = summed[6, 0]

    out_g, out_d, out_m, out_v = {}, {}, {}, {}
    for n in ORDER:
        shape = weights[n].shape
        two_d = shape[1:] if len(shape) == 3 else (1, weights[n].size)
        g2 = grads[n].reshape(two_d)
        wmv = adam_in if n == "w_in" else [t.reshape(two_d) for t in (weights[n], m_in[n], v_in[n])]
        d2, m2, v2 = _adamw(wmv[0], g2, wmv[1], wmv[2], name=f"adamw_{n}")
        out_g[n], out_d[n], out_m[n], out_v[n] = (g2.reshape(shape), d2.reshape(shape), m2.reshape(shape),
                                                  v2.reshape(shape))
    return (loss, grad_x[None], *[out_g[n] for n in ORDER], *[out_d[n] for n in ORDER],
            *[out_m[n] for n in ORDER], *[out_v[n] for n in ORDER])
```

```python
import numpy as np
import jax
import jax.numpy as jnp
from jax import lax
from jax.experimental import pallas as pl
from jax.experimental.pallas import tpu as pltpu

F32 = jnp.float32
_CD = jnp.bfloat16

D_MODEL = 1024
HEAD_DIM = 64
DIL_PAIRS = ((128, 1), (512, 4), (2048, 16))
N_DIL_GROUPS = 3
DIL_HEADS = 4
DIL_W = 128
DIL_OUT = DIL_HEADS * HEAD_DIM
DIL_WIDTH = N_DIL_GROUPS * DIL_OUT
N_FOX_HEADS = 8
FOX_WIDTH = N_FOX_HEADS * HEAD_DIM
D_FF = 2816
QKV_COLS = 3 * DIL_WIDTH + 3 * FOX_WIDTH
F_PAD = 128
RMS_EPS = 1e-6
NEG_INF = -1e30
ATTN_SCALE = HEAD_DIM ** -0.5
ADAM_LR, ADAM_B1, ADAM_B2, ADAM_EPS, ADAM_WD, ADAM_STEP = 0.001, 0.9, 0.999, 1e-08, 0.01, 10

VMEM_LIMIT = 48 * 1024 * 1024
VMEM_LIMIT_RESIDENT = 56 * 1024 * 1024
LANES = 128
MESH = pl.DeviceIdType.MESH
HBM_SPEC = pl.BlockSpec(memory_space=pltpu.HBM)


def _pcall(body, after=None, **kw):
    if after is None:
        return pl.pallas_call(body, **kw)
    n_in = len(kw["in_specs"])
    kw["in_specs"] = list(kw["in_specs"]) + [pl.BlockSpec(memory_space=pl.ANY)]

    def tied(*refs):
        return body(*refs[:n_in], *refs[n_in + 1:])

    call = pl.pallas_call(tied, **kw)
    return lambda *args: call(*args, after)


def _params(*sem):
    return pltpu.CompilerParams(dimension_semantics=sem, vmem_limit_bytes=VMEM_LIMIT)


def _pick(dim, pref):
    t = (min(pref, dim) // 128) * 128
    while t >= 128:
        if dim % t == 0:
            return t
        t -= 128
    return dim


def _mm(a, b, *, name, ta=False, tb=False, out_dtype=F32, add=None, tm=1024, tn=512, tk=2048, after=None,
        b_blocks=False, out_blocks=None, a_halves=False, b_halves=False):
    if a_halves:
        M, K = a.shape[1], 2 * a.shape[2]
    elif ta:
        K, M = a.shape
    else:
        M, K = a.shape
    if b_halves:
        b_rows, b_cols = b.shape[1], 2 * b.shape[2]
    else:
        b_rows, b_cols = (b.shape[1], b.shape[0] * b.shape[2]) if b_blocks else b.shape
    if tb:
        N, K2 = b_rows, b_cols
    else:
        K2, N = b_rows, b_cols
    assert K == K2, (a.shape, b.shape)
    shard = b.shape[2] if b_blocks else None
    tm = _pick(M, tm)
    tn = _pick(shard if (b_blocks and not tb) else (out_blocks or N), tn)
    tk = _pick(shard if (b_blocks and tb) else K, tk)
    nk = K // tk
    dn = (((0 if ta else 1,), (1 if tb else 0,)), ((), ()))
    has_add = add is not None
    assert not (has_add and out_blocks)

    def body(*refs):
        a_ref, b_ref = refs[0], refs[1]
        add_ref = refs[2] if has_add else None
        o_ref = refs[3] if has_add else refs[2]
        bv = b_ref[0] if b_blocks else b_ref[...]
        p = lax.dot_general(a_ref[...].astype(_CD), bv.astype(_CD), dn, preferred_element_type=F32)

        def finish(r):
            if has_add:
                r = r + add_ref[...]
            if out_blocks:
                o_ref[0] = r.astype(out_dtype)
            else:
                o_ref[...] = r.astype(out_dtype)

        if nk == 1:
            finish(p)
        else:
            acc_ref = refs[-1]
            k = pl.program_id(2)

            @pl.when(k == 0)
            def _():
                acc_ref[...] = p

            @pl.when(k > 0)
            def _():
                acc_ref[...] += p

            @pl.when(k == nk - 1)
            def _():
                finish(acc_ref[...])

    if a_halves:
        ka = (K // 2) // tk
        a_spec = pl.BlockSpec((None, tm, tk), lambda i, j, k: (k // ka, i, k % ka))
    else:
        a_spec = pl.BlockSpec((tk, tm), lambda i, j, k: (k, i)) if ta else pl.BlockSpec((tm, tk), lambda i, j, k: (i, k))
    if b_halves:
        nb_ = (N // 2) // tn
        b_spec = pl.BlockSpec((None, tk, tn), lambda i, j, k: (j // nb_, k, j % nb_))
    elif b_blocks and tb:
        per = shard // tk
        b_spec = pl.BlockSpec((1, tn, tk), lambda i, j, k: (k // per, j, k % per))
    elif b_blocks:
        per = shard // tn
        b_spec = pl.BlockSpec((1, tk, tn), lambda i, j, k: (j // per, k, j % per))
    else:
        b_spec = pl.BlockSpec((tn, tk), lambda i, j, k: (j, k)) if tb else pl.BlockSpec((tk, tn), lambda i, j, k: (k, j))
    if out_blocks:
        oper = out_blocks // tn
        o_spec = pl.BlockSpec((1, tm, tn), lambda i, j, k: (j // oper, i, j % oper))
        out_shape = jax.ShapeDtypeStruct((N // out_blocks, M, out_blocks), out_dtype)
    else:
        o_spec = pl.BlockSpec((tm, tn), lambda i, j, k: (i, j))
        out_shape = jax.ShapeDtypeStruct((M, N), out_dtype)
    in_specs = [a_spec, b_spec] + ([o_spec] if has_add else [])
    args = (a, b) + ((add,) if has_add else ())
    return _pcall(
        body, after, name=name, grid=(M // tm, N // tn, nk), in_specs=in_specs, out_specs=o_spec,
        out_shape=out_shape,
        scratch_shapes=[pltpu.VMEM((tm, tn), F32)] if nk > 1 else [],
        compiler_params=_params("parallel", "parallel", "arbitrary"),
    )(*args)


def _rms_fwd(x, g, *, name, tm=512, after=None):
    S, D = x.shape

    def body(x_ref, g_ref, h_ref):
        xv = x_ref[...]
        r = lax.rsqrt(jnp.mean(xv * xv, axis=-1, keepdims=True) + RMS_EPS)
        h_ref[...] = ((xv * r) * g_ref[...]).astype(h_ref.dtype)

    row = pl.BlockSpec((tm, D), lambda i: (i, 0))
    return _pcall(body, after, name=name, grid=(S // tm,), in_specs=[row, pl.BlockSpec((1, D), lambda i: (0, 0))],
                  out_specs=row, out_shape=jax.ShapeDtypeStruct((S, D), _CD), compiler_params=_params("parallel"))(x, g)


def _rms_bwd(x, g, dh, dres, *, name, tm=512, after=None):
    S, D = x.shape

    def body(x_ref, g_ref, dh_ref, dres_ref, dx_ref, dg_ref):
        xv = x_ref[...]
        r = lax.rsqrt(jnp.mean(xv * xv, axis=-1, keepdims=True) + RMS_EPS)
        xh = xv * r
        dhv = dh_ref[...]
        dxh = dhv * g_ref[...]
        dx_ref[...] = dres_ref[...] + r * (dxh - xh * jnp.mean(dxh * xh, axis=-1, keepdims=True))
        part = jnp.sum(dhv * xh, axis=0, keepdims=True)

        @pl.when(pl.program_id(0) == 0)
        def _():
            dg_ref[...] = part

        @pl.when(pl.program_id(0) > 0)
        def _():
            dg_ref[...] += part

    row = pl.BlockSpec((tm, D), lambda i: (i, 0))
    vec = pl.BlockSpec((1, D), lambda i: (0, 0))
    return _pcall(body, after, name=name, grid=(S // tm,), in_specs=[row, vec, row, row], out_specs=[row, vec],
                  out_shape=[jax.ShapeDtypeStruct((S, D), F32), jax.ShapeDtypeStruct((1, D), F32)],
                  compiler_params=_params("arbitrary"))(x, g, dh, dres)


def _loss_head(x, g, tgt, *, name, tm=512):
    S, D = x.shape

    def body(x_ref, g_ref, t_ref, loss_ref, dx_ref, dg_ref):
        xv = x_ref[...]
        gv = g_ref[...]
        r = lax.rsqrt(jnp.mean(xv * xv, axis=-1, keepdims=True) + RMS_EPS)
        xh = xv * r
        err = xh * gv - t_ref[...]
        lpart = 0.5 * jnp.sum(jnp.mean(err * err, axis=-1, keepdims=True), axis=0, keepdims=True)
        dy = err * (1.0 / D)
        dxh = dy * gv
        dx_ref[...] = r * (dxh - xh * jnp.mean(dxh * xh, axis=-1, keepdims=True))
        gpart = jnp.sum(dy * xh, axis=0, keepdims=True)

        @pl.when(pl.program_id(0) == 0)
        def _():
            loss_ref[...] = lpart
            dg_ref[...] = gpart

        @pl.when(pl.program_id(0) > 0)
        def _():
            loss_ref[...] += lpart
            dg_ref[...] += gpart

    row = pl.BlockSpec((tm, D), lambda i: (i, 0))
    vec = pl.BlockSpec((1, D), lambda i: (0, 0))
    one = pl.BlockSpec((1, 1), lambda i: (0, 0))
    return _pcall(body, name=name, grid=(S // tm,), in_specs=[row, vec, row], out_specs=[one, row, vec],
                  out_shape=[jax.ShapeDtypeStruct((1, 1), F32), jax.ShapeDtypeStruct((S, D), F32),
                             jax.ShapeDtypeStruct((1, D), F32)],
                  compiler_params=_params("arbitrary"))(x, g, tgt)


def _sigmoid(z):
    return 1.0 / (1.0 + jnp.exp(-z))


def _gate_fwd(gl, bg, ya, yb, *, name, tm=512):
    S, D = ya.shape

    def body(za_ref, zb_ref, ba_ref, bb_ref, ya_ref, yb_ref, o_ref):
        ga = _sigmoid(za_ref[...].astype(F32) + ba_ref[...])
        gb = _sigmoid(zb_ref[...].astype(F32) + bb_ref[...])
        o_ref[...] = (ga * ya_ref[...].astype(F32) + gb * yb_ref[...].astype(F32)).astype(o_ref.dtype)

    lo = pl.BlockSpec((tm, D), lambda i: (i, 0))
    hi = pl.BlockSpec((tm, D), lambda i: (i, 1))
    vlo = pl.BlockSpec((1, D), lambda i: (0, 0))
    vhi = pl.BlockSpec((1, D), lambda i: (0, 1))
    return _pcall(body, name=name, grid=(S // tm,), in_specs=[lo, hi, vlo, vhi, lo, lo], out_specs=lo,
                  out_shape=jax.ShapeDtypeStruct((S, D), _CD), compiler_params=_params("parallel"))(gl, gl, bg, bg, ya, yb)


def _gate_bwd(dm, gl, bg, ya, yb, *, name, tm=256):
    S, D = ya.shape

    def body(dm_ref, za_ref, zb_ref, ba_ref, bb_ref, ya_ref, yb_ref, dya_ref, dyb_ref, dgl_ref, dbg_ref):
        dmv = dm_ref[...].astype(F32)
        ga = _sigmoid(za_ref[...].astype(F32) + ba_ref[...])
        gb = _sigmoid(zb_ref[...].astype(F32) + bb_ref[...])
        dya_ref[...] = (dmv * ga).astype(dya_ref.dtype)
        dyb_ref[...] = (dmv * gb).astype(dyb_ref.dtype)
        dza = dmv * ya_ref[...].astype(F32) * ga * (1.0 - ga)
        dzb = dmv * yb_ref[...].astype(F32) * gb * (1.0 - gb)
        dgl_ref[:, :D] = dza.astype(dgl_ref.dtype)
        dgl_ref[:, D:] = dzb.astype(dgl_ref.dtype)
        pa = jnp.sum(dza, axis=0, keepdims=True)
        pb = jnp.sum(dzb, axis=0, keepdims=True)

        @pl.when(pl.program_id(0) == 0)
        def _():
            dbg_ref[:, :D] = pa
            dbg_ref[:, D:] = pb

        @pl.when(pl.program_id(0) > 0)
        def _():
            dbg_ref[:, :D] += pa
            dbg_ref[:, D:] += pb

    lo = pl.BlockSpec((tm, D), lambda i: (i, 0))
    hi = pl.BlockSpec((tm, D), lambda i: (i, 1))
    vlo = pl.BlockSpec((1, D), lambda i: (0, 0))
    vhi = pl.BlockSpec((1, D), lambda i: (0, 1))
    wide = pl.BlockSpec((tm, 2 * D), lambda i: (i, 0))
    vwide = pl.BlockSpec((1, 2 * D), lambda i: (0, 0))
    return _pcall(body, name=name, grid=(S // tm,), in_specs=[lo, lo, hi, vlo, vhi, lo, lo],
                  out_specs=[lo, lo, wide, vwide],
                  out_shape=[jax.ShapeDtypeStruct((S, D), _CD), jax.ShapeDtypeStruct((S, D), _CD),
                             jax.ShapeDtypeStruct((S, 2 * D), _CD), jax.ShapeDtypeStruct((1, 2 * D), F32)],
                  compiler_params=_params("arbitrary"))(dm, gl, gl, bg, bg, ya, yb)


def _ffn_in_act(h2, w_blocks, *, name, tm=512):
    S, D = h2.shape
    _, _, C = w_blocks.shape

    def body(a_ref, bg_ref, bu_ref, g_ref, u_ref, o_ref):
        av = a_ref[...].astype(_CD)
        gv = jnp.dot(av, bg_ref[0].astype(_CD), preferred_element_type=F32)
        uv = jnp.dot(av, bu_ref[0].astype(_CD), preferred_element_type=F32)
        g_ref[...] = gv.astype(g_ref.dtype)
        u_ref[...] = uv.astype(u_ref.dtype)
        o_ref[...] = (gv * _sigmoid(gv) * uv).astype(o_ref.dtype)

    out = pl.BlockSpec((tm, C), lambda i, j: (i, j))
    shp = jax.ShapeDtypeStruct((S, 2 * C), _CD)
    return _pcall(body, name=name, grid=(S // tm, 2),
                  in_specs=[pl.BlockSpec((tm, D), lambda i, j: (i, 0)), pl.BlockSpec((1, D, C), lambda i, j: (j, 0, 0)),
                            pl.BlockSpec((1, D, C), lambda i, j: (2 + j, 0, 0))],
                  out_specs=[out, out, out], out_shape=[shp, shp, shp],
                  compiler_params=_params("parallel", "arbitrary"))(h2, w_blocks, w_blocks)


def _d_swiglu(dx, w_down, gate, up, *, name, tm=512, tn=1408):
    S, D = dx.shape
    F = w_down.shape[0]
    nt = (((1,), (1,)), ((), ()))

    def body(a_ref, b_ref, g_ref, u_ref, o_ref):
        dv = lax.dot_general(a_ref[...].astype(_CD), b_ref[...].astype(_CD), nt, preferred_element_type=F32)
        gv = g_ref[...].astype(F32)
        sg = _sigmoid(gv)
        o_ref[0] = (dv * u_ref[...].astype(F32) * (sg * (1.0 + gv * (1.0 - sg)))).astype(o_ref.dtype)
        o_ref[1] = (dv * (gv * sg)).astype(o_ref.dtype)

    tile = pl.BlockSpec((tm, tn), lambda i, j: (i, j))
    return _pcall(body, name=name, grid=(S // tm, F // tn),
                  in_specs=[pl.BlockSpec((tm, D), lambda i, j: (i, 0)), pl.BlockSpec((tn, D), lambda i, j: (j, 0)),
                            tile, tile],
                  out_specs=pl.BlockSpec((2, tm, tn), lambda i, j: (0, i, j)),
                  out_shape=jax.ShapeDtypeStruct((2, S, F), _CD),
                  compiler_params=_params("parallel", "arbitrary"))(dx, w_down, gate, up)


def _split3(x):
    hi = x.astype(jnp.bfloat16)
    r1 = x - hi.astype(F32)
    mid = r1.astype(jnp.bfloat16)
    lo = (r1 - mid.astype(F32)).astype(jnp.bfloat16)
    return hi, mid, lo


def _ones_dot_left(ones, x):
    return sum(jnp.dot(ones, p, preferred_element_type=F32) for p in _split3(x))


def _ones_dot_right(x, ones):
    return sum(jnp.dot(p, ones, preferred_element_type=F32) for p in _split3(x))


def _head_sum(x):
    n = x.shape[1]
    r = lax.broadcasted_iota(jnp.int32, (n, n), 0) // HEAD_DIM
    c = lax.broadcasted_iota(jnp.int32, (n, n), 1) // HEAD_DIM
    return _ones_dot_right(x, (r == c).astype(jnp.bfloat16))


def _log_sigmoid(z):
    e = jnp.exp(-jnp.abs(z))
    t = 1.0 + e
    log1p_e = jnp.where(t == 1.0, e, jnp.log(t) * (e / jnp.where(t == 1.0, 1.0, t - 1.0)))
    return jnp.minimum(z, 0.0) - log1p_e


def _fox_cumsum(zf, bf, *, name):
    S, W = zf.shape
    nb = S // 128

    def body(z_ref, b_ref, c_ref):
        tri = (lax.broadcasted_iota(jnp.int32, (128, 128), 0) >= lax.broadcasted_iota(jnp.int32, (128, 128), 1))
        tri = tri.astype(jnp.bfloat16)

        def step(i, carry):
            rows = pl.ds(pl.multiple_of(i * 128, 128), 128)
            lf = _log_sigmoid(z_ref[rows, :] + b_ref[...])
            cb = _ones_dot_left(tri, lf) + carry
            c_ref[rows, :] = cb
            return cb[127:128, :]

        lax.fori_loop(0, nb, step, jnp.zeros((1, W), F32))

    return _pcall(body, name=name, out_shape=jax.ShapeDtypeStruct((S, W), F32),
                  compiler_params=pltpu.CompilerParams(vmem_limit_bytes=VMEM_LIMIT))(zf, bf)


def _fox_cumsum_bwd(dc, zf, bf, *, name):
    S, W = zf.shape
    nb = S // 128

    def body(dc_ref, z_ref, b_ref, dz_ref, db_ref):
        tri = (lax.broadcasted_iota(jnp.int32, (128, 128), 0) <= lax.broadcasted_iota(jnp.int32, (128, 128), 1))
        tri = tri.astype(jnp.bfloat16)

        def step(k, carry):
            tail, acc = carry
            i = nb - 1 - k
            rows = pl.ds(pl.multiple_of(i * 128, 128), 128)
            dlf = _ones_dot_left(tri, dc_ref[rows, :]) + tail
            dz = dlf * _sigmoid(-(z_ref[rows, :] + b_ref[...]))
            dz_ref[rows, :] = dz
            return dlf[0:1, :], acc + jnp.sum(dz, axis=0, keepdims=True)

        _, acc = lax.fori_loop(0, nb, step, (jnp.zeros((1, W), F32), jnp.zeros((1, W), F32)))
        db_ref[...] = acc

    return _pcall(body, name=name,
                  out_shape=[jax.ShapeDtypeStruct((S, W), F32), jax.ShapeDtypeStruct((1, W), F32)],
                  compiler_params=pltpu.CompilerParams(vmem_limit_bytes=VMEM_LIMIT))(dc, zf, bf)


def _proj_dil(h, w_qkv, *, name, tm=1024):
    S, D = h.shape
    tn = DIL_WIDTH

    def body(a_ref, b_ref, *rest):
        outs, acc = rest[:N_DIL_GROUPS], rest[N_DIL_GROUPS]
        prod = jnp.dot(a_ref[...].astype(_CD), b_ref[...].astype(_CD), preferred_element_type=F32)
        for k in range(tn // LANES):
            acc[k] = prod[:, k * LANES:(k + 1) * LANES]
        for g, (_, d) in enumerate(DIL_PAIRS):
            for half in range(DIL_OUT // LANES):
                k = g * (DIL_OUT // LANES) + half
                cols = slice(half * LANES, (half + 1) * LANES)
                for r in range(d):
                    rows = pl.ds(r, tm // d, stride=d) if d > 1 else slice(None)
                    outs[g][0, r, :, cols] = acc[k, rows, :].astype(outs[g].dtype)

    out_specs = [pl.BlockSpec((1, d, tm // d, DIL_OUT), lambda i, j: (j, 0, i, 0)) for _, d in DIL_PAIRS]
    out_shape = [jax.ShapeDtypeStruct((3, d, S // d, DIL_OUT), _CD) for _, d in DIL_PAIRS]
    outs = _pcall(body, name=name, grid=(S // tm, 3),
                  in_specs=[pl.BlockSpec((tm, D), lambda i, j: (i, 0)), pl.BlockSpec((D, tn), lambda i, j: (0, j))],
                  out_specs=out_specs, out_shape=out_shape, scratch_shapes=[pltpu.VMEM((tn // LANES, tm, LANES), F32)],
                  compiler_params=_params("parallel", "arbitrary"))(h, w_qkv)
    return [o.reshape(3, S, DIL_OUT) for o in outs]


def _dil_start(block, S, dilation):
    sub = S // dilation
    u0 = block * DIL_W
    return (u0 % sub) * dilation + u0 // sub


def _dil_slopes(group):
    h = np.arange(1, N_DIL_GROUPS * DIL_HEADS + 1, dtype=np.float32)
    s = (np.float32(2.0) ** (np.float32(-8.0) * h / np.float32(N_DIL_GROUPS * DIL_HEADS))).astype(np.float32)
    return [float(v) for v in s.reshape(N_DIL_GROUPS, DIL_HEADS)[group]]


def _dil_tiles(i, n, blocks_per_seq):
    qi = lax.broadcasted_iota(jnp.int32, (DIL_W, 2 * DIL_W), 0)
    kj = lax.broadcasted_iota(jnp.int32, (DIL_W, 2 * DIL_W), 1)
    rel = qi + DIL_W - kj
    first = ((4 * n + i) % blocks_per_seq) == 0
    valid = jnp.logical_and(jnp.logical_and(rel >= 0, rel <= DIL_W), jnp.logical_or(kj >= DIL_W, jnp.logical_not(first)))
    return valid, rel.astype(F32)


def _dil_window(cur_ref, prev_ref, i, cols):
    if i > 0:
        return cur_ref[(i - 1) * DIL_W:(i + 1) * DIL_W, cols]
    return jnp.concatenate([prev_ref[:, cols], cur_ref[:DIL_W, cols]], axis=0)


CHUNK = 4 * DIL_W


def _dil_rows(block, S, dilation):
    start = _dil_start(block, S, dilation)
    return pl.ds(start, DIL_W, stride=dilation) if dilation > 1 else pl.ds(start, DIL_W)


def SPLIT(S):
    return (DIL_OUT // LANES, S, LANES)


def _dil_fwd(qkv, group, *, name):
    S = qkv.shape[1]
    dilation = DIL_PAIRS[group][1]
    bps = (S // dilation) // DIL_W
    slopes = _dil_slopes(group)
    nt = (((1,), (1,)), ((), ()))

    def body(q_ref, k_ref, v_ref, kp_ref, vp_ref, on_ref, ln_ref, o_ref, l_ref):
        n = pl.program_id(0)
        for i in range(4):
            valid, rel = _dil_tiles(i, n, bps)
            rows = slice(i * DIL_W, (i + 1) * DIL_W)
            for h in range(DIL_HEADS):
                cols = slice(h * HEAD_DIM, (h + 1) * HEAD_DIM)
                qh = q_ref[rows, cols]
                k2, v2 = _dil_window(k_ref, kp_ref, i, cols), _dil_window(v_ref, vp_ref, i, cols)
                s = lax.dot_general(qh, k2, nt, preferred_element_type=F32) * ATTN_SCALE - (slopes[h] * dilation) * rel
                s = jnp.where(valid, s, NEG_INF)
                m = jnp.max(s, axis=-1, keepdims=True)
                p = jnp.exp(s - m)
                den = jnp.sum(p, axis=-1, keepdims=True)
                acc = jnp.dot(p.astype(_CD), v2, preferred_element_type=F32)
                o_ref[rows, cols] = acc / den
                l_ref[rows, cols] = jnp.broadcast_to(m + jnp.log(den), (DIL_W, HEAD_DIM))
        for i in range(4):
            rows = slice(i * DIL_W, (i + 1) * DIL_W)
            nat = _dil_rows(4 * n + i, S, dilation)
            for half in range(DIL_OUT // LANES):
                cols = slice(half * LANES, (half + 1) * LANES)
                on_ref[half, nat, :] = o_ref[rows, cols]
                ln_ref[half, nat, :] = l_ref[rows, cols]

    def cur(which):
        return pl.BlockSpec((None, CHUNK, DIL_OUT), lambda n: (which, n, 0))

    def prev(which):
        return pl.BlockSpec((None, DIL_W, DIL_OUT), lambda n: (which, jnp.maximum(4 * n - 1, 0), 0))

    whole = pl.BlockSpec(SPLIT(S), lambda n: (0, 0, 0))
    return _pcall(body, name=name, grid=(S // CHUNK,), in_specs=[cur(0), cur(1), cur(2), prev(1), prev(2)],
                  out_specs=[whole, whole],
                  out_shape=[jax.ShapeDtypeStruct(SPLIT(S), F32), jax.ShapeDtypeStruct(SPLIT(S), F32)],
                  scratch_shapes=[pltpu.VMEM((CHUNK, DIL_OUT), F32), pltpu.VMEM((CHUNK, DIL_OUT), F32)],
                  compiler_params=_params("arbitrary"))(qkv, qkv, qkv, qkv, qkv)


STAT_OFFSET = HEAD_DIM // 2


def _dil_bwd(qkv, stats, do, group, *, name):
    S = qkv.shape[1]
    dilation = DIL_PAIRS[group][1]
    bps = (S // dilation) // DIL_W
    slopes = _dil_slopes(group)
    nchunk = S // CHUNK
    nt = (((1,), (1,)), ((), ()))
    tn = (((0,), (0,)), ((), ()))

    def body(q_ref, k_ref, v_ref, kp_ref, vp_ref, ln_ref, don_ref, dqn_ref, dkn_ref, dvn_ref,
             dk_s, dv_s, l_ref, do_ref, dq_ref):
        step = pl.program_id(0)
        n = nchunk - 1 - step
        for i in range(4):
            rows = slice(i * DIL_W, (i + 1) * DIL_W)
            nat = _dil_rows(4 * n + i, S, dilation)
            for half in range(DIL_OUT // LANES):
                cols = slice(half * LANES, (half + 1) * LANES)
                l_ref[rows, cols] = ln_ref[half, nat, :]
                do_ref[rows, cols] = don_ref[half, nat, :]

        @pl.when(step == 0)
        def _():
            dk_s[:, CHUNK:] = jnp.zeros((DIL_OUT, DIL_W), F32)
            dv_s[:, CHUNK:] = jnp.zeros((DIL_OUT, DIL_W), F32)

        dk_s[:, :CHUNK] = jnp.zeros((DIL_OUT, CHUNK), F32)
        dv_s[:, :CHUNK] = jnp.zeros((DIL_OUT, CHUNK), F32)
        for i in range(4):
            valid, rel = _dil_tiles(i, n, bps)
            rows = slice(i * DIL_W, (i + 1) * DIL_W)
            window = slice(i * DIL_W, (i + 2) * DIL_W)
            for h in range(DIL_HEADS):
                cols = slice(h * HEAD_DIM, (h + 1) * HEAD_DIM)
                qh = q_ref[rows, cols]
                k2, v2 = _dil_window(k_ref, kp_ref, i, cols), _dil_window(v_ref, vp_ref, i, cols)
                lh = l_ref[rows, h * HEAD_DIM:h * HEAD_DIM + 1]
                shift = l_ref[rows, h * HEAD_DIM + STAT_OFFSET:h * HEAD_DIM + STAT_OFFSET + 1]
                s = lax.dot_general(qh, k2, nt, preferred_element_type=F32) * ATTN_SCALE - (slopes[h] * dilation) * rel
                p = jnp.exp(jnp.where(valid, s, NEG_INF) - lh)
                dob = do_ref[rows, cols].astype(_CD)
                ds = p * (lax.dot_general(dob, v2, nt, preferred_element_type=F32) + shift)
                dsb = (ds * ATTN_SCALE).astype(_CD)
                dq_ref[rows, cols] = jnp.dot(dsb, k2, preferred_element_type=F32)
                dk_s[cols, window] += lax.dot_general(qh, dsb, tn, preferred_element_type=F32)
                dv_s[cols, window] += lax.dot_general(dob, p.astype(_CD), tn, preferred_element_type=F32)
        for i in range(4):
            rows = slice(i * DIL_W, (i + 1) * DIL_W)
            done = slice((i + 1) * DIL_W, (i + 2) * DIL_W)
            nat = _dil_rows(4 * n + i, S, dilation)
            dkb, dvb = dk_s[:, done].T, dv_s[:, done].T
            for half in range(DIL_OUT // LANES):
                cols = slice(half * LANES, (half + 1) * LANES)
                dqn_ref[half, nat, :] = dq_ref[rows, cols]
                dkn_ref[half, nat, :] = dkb[:, cols]
                dvn_ref[half, nat, :] = dvb[:, cols]
        dk_s[:, CHUNK:] = dk_s[:, :DIL_W]
        dv_s[:, CHUNK:] = dv_s[:, :DIL_W]

    def cur(which):
        return pl.BlockSpec((None, CHUNK, DIL_OUT), lambda s: (which, nchunk - 1 - s, 0))

    def prev(which):
        return pl.BlockSpec((None, DIL_W, DIL_OUT), lambda s: (which, jnp.maximum(4 * (nchunk - 1 - s) - 1, 0), 0))

    whole = pl.BlockSpec(SPLIT(S), lambda s: (0, 0, 0))
    shp = jax.ShapeDtypeStruct(SPLIT(S), F32)
    tile = pltpu.VMEM((CHUNK, DIL_OUT), F32)
    return _pcall(body, name=name, grid=(nchunk,),
                  in_specs=[cur(0), cur(1), cur(2), prev(1), prev(2), whole, whole],
                  out_specs=[whole, whole, whole], out_shape=[shp, shp, shp],
                  scratch_shapes=[pltpu.VMEM((DIL_OUT, CHUNK + DIL_W), F32), pltpu.VMEM((DIL_OUT, CHUNK + DIL_W), F32),
                                  tile, tile, tile],
                  compiler_params=pltpu.CompilerParams(dimension_semantics=("arbitrary",),
                                                       vmem_limit_bytes=VMEM_LIMIT_RESIDENT))(
        qkv, qkv, qkv, qkv, qkv, stats, do)


def _dil_mix_fwd(os_, ls_, *, name, tm=512):
    nh, S, _ = os_[0].shape

    def body(o0, o1, o2, l0, l1, l2, out_ref):
        for half in range(nh):
            ls = [l0[half], l1[half], l2[half]]
            m = jnp.maximum(jnp.maximum(ls[0], ls[1]), ls[2])
            es = [jnp.exp(l - m) for l in ls]
            den = es[0] + es[1] + es[2]
            mixed = (es[0] * o0[half] + es[1] * o1[half] + es[2] * o2[half]) / den
            out_ref[:, half * LANES:(half + 1) * LANES] = mixed.astype(out_ref.dtype)

    halves = pl.BlockSpec((nh, tm, LANES), lambda i: (0, i, 0))
    row = pl.BlockSpec((tm, nh * LANES), lambda i: (i, 0))
    return _pcall(body, name=name, grid=(S // tm,), in_specs=[halves] * 6, out_specs=row,
                  out_shape=jax.ShapeDtypeStruct((S, nh * LANES), _CD), compiler_params=_params("parallel"))(*os_, *ls_)


def _dil_mix_bwd(doa, os_, ls_, *, name, tm=512, after=None):
    nh, S, _ = os_[0].shape

    def body(d_ref, o0, o1, o2, l0, l1, l2, do0, do1, do2, st0, st1, st2):
        first = lax.broadcasted_iota(jnp.int32, (tm, LANES), 1) % HEAD_DIM < STAT_OFFSET
        for half in range(nh):
            dv = d_ref[:, half * LANES:(half + 1) * LANES]
            ls = [l0[half], l1[half], l2[half]]
            m = jnp.maximum(jnp.maximum(ls[0], ls[1]), ls[2])
            es = [jnp.exp(l - m) for l in ls]
            den = es[0] + es[1] + es[2]
            al = [e / den for e in es]
            da = [_head_sum(dv * o[half]) for o in (o0, o1, o2)]
            mean = al[0] * da[0] + al[1] * da[1] + al[2] * da[2]
            for a, l, do_ref, st_ref in zip(al, ls, (do0, do1, do2), (st0, st1, st2)):
                do_ref[half] = a * dv
                st_ref[half] = jnp.where(first, l, -a * mean)

    halves = pl.BlockSpec((nh, tm, LANES), lambda i: (0, i, 0))
    row = pl.BlockSpec((tm, nh * LANES), lambda i: (i, 0))
    shp = jax.ShapeDtypeStruct((nh, S, LANES), F32)
    return _pcall(body, after, name=name, grid=(S // tm,), in_specs=[row] + [halves] * 6, out_specs=[halves] * 6,
                  out_shape=[shp] * 6, compiler_params=_params("parallel"))(doa, *os_, *ls_)


FOX_T = 512


PACK = 2 * HEAD_DIM
HEAD_PAIRS = N_FOX_HEADS // 2
FOX_HPS = 8
Q_BLOCK0 = 0
K_BLOCK0 = FOX_WIDTH // PACK
V_BLOCK0 = 2 * FOX_WIDTH // PACK


def _pieces(x):
    hi = x.astype(jnp.bfloat16).astype(F32)
    r = x - hi
    mid = r.astype(jnp.bfloat16).astype(F32)
    lo = (r - mid).astype(jnp.bfloat16).astype(F32)
    return [hi, mid, lo]


def _extras(first, second, rows):
    lane = lax.broadcasted_iota(jnp.int32, (rows, HEAD_DIM), 1)
    out = jnp.zeros((rows, HEAD_DIM), F32)
    for base, triple in ((0, first), (3, second)):
        if all(isinstance(v, float) for v in triple) and len(set(triple)) == 1:
            if triple[0] != 0.0:
                out = jnp.where(jnp.logical_and(lane >= base, lane < base + 3), triple[0], out)
        else:
            for idx, val in enumerate(triple):
                out = jnp.where(lane == base + idx, val, out)
    return out


def _head_column(c, h):
    lane = lax.broadcasted_iota(jnp.int32, c.shape, 1)
    return jnp.sum(jnp.where(lane == h, c, 0.0), axis=1, keepdims=True)


ONES3 = [1.0, 1.0, 1.0]
ZEROS3 = [0.0, 0.0, 0.0]


def _fox_pack_fwd(qkv, c, *, name, tm=512):
    S = qkv.shape[0]

    def body(q_ref, k_ref, v_ref, c_ref, qo_ref, ko_ref, vo_ref):
        hp = pl.program_id(1)
        cv = c_ref[...]
        v_extras = jnp.where(lax.broadcasted_iota(jnp.int32, (tm, HEAD_DIM), 1) < 3, 1.0, 0.0).astype(vo_ref.dtype)
        for hh in range(2):
            ch = _pieces(_head_column(cv, 2 * hp + hh))
            src = slice(hh * HEAD_DIM, (hh + 1) * HEAD_DIM)
            lo = slice(hh * PACK, hh * PACK + HEAD_DIM)
            hi = slice(hh * PACK + HEAD_DIM, (hh + 1) * PACK)
            qo_ref[:, lo] = (q_ref[:, src].astype(F32) * ATTN_SCALE).astype(qo_ref.dtype)
            qo_ref[:, hi] = _extras(ch, ONES3, tm).astype(qo_ref.dtype)
            ko_ref[:, lo] = k_ref[:, src]
            ko_ref[:, hi] = _extras(ONES3, [-p for p in ch], tm).astype(ko_ref.dtype)
            vo_ref[:, lo] = v_ref[:, src]
            vo_ref[:, hi] = v_extras

    def src(block0):
        return pl.BlockSpec((tm, PACK), lambda i, hp: (i, block0 + hp))

    out = pl.BlockSpec((tm, 2 * PACK), lambda i, hp: (i, hp))
    shp = jax.ShapeDtypeStruct((S, N_FOX_HEADS * PACK), _CD)
    return _pcall(body, name=name, grid=(S // tm, HEAD_PAIRS),
                  in_specs=[src(Q_BLOCK0), src(K_BLOCK0), src(V_BLOCK0), pl.BlockSpec((tm, PACK), lambda i, hp: (i, 0))],
                  out_specs=[out, out, out], out_shape=[shp, shp, shp],
                  compiler_params=_params("parallel", "parallel"))(qkv, qkv, qkv, c)


def _fox_fwd(qp, kp, vp, *, name):
    S = qp.shape[0]
    nt = S // FOX_T
    nt_dims = (((1,), (1,)), ((), ()))
    tn_dims = (((0,), (0,)), ((), ()))

    def body(i_tab, j_tab, q_ref, k_ref, v_ref, o_ref, l_ref, m_s, acc_s):
        t = pl.program_id(1)
        i, j = i_tab[t], j_tab[t]

        @pl.when(j == 0)
        def _():
            m_s[...] = jnp.full((FOX_HPS, 1, FOX_T), NEG_INF, F32)
            acc_s[...] = jnp.zeros((FOX_HPS, PACK, FOX_T), F32)

        def tile(diagonal):
            for hh in range(FOX_HPS):
                cols = slice(hh * PACK, (hh + 1) * PACK)
                st = lax.dot_general(k_ref[:, cols], q_ref[:, cols], nt_dims, preferred_element_type=F32)
                if diagonal:
                    key = lax.broadcasted_iota(jnp.int32, (FOX_T, FOX_T), 0)
                    qry = lax.broadcasted_iota(jnp.int32, (FOX_T, FOX_T), 1)
                    st = jnp.where(key <= qry, st, NEG_INF)
                m_old = m_s[hh]
                m_new = jnp.maximum(m_old, jnp.max(st, axis=0, keepdims=True))
                pt = jnp.exp(st - m_new)
                acc_s[hh] = jnp.exp(m_old - m_new) * acc_s[hh] + lax.dot_general(
                    v_ref[:, cols], pt.astype(_CD), tn_dims, preferred_element_type=F32)
                m_s[hh] = m_new

        @pl.when(j < i)
        def _():
            tile(False)

        @pl.when(j == i)
        def _():
            tile(True)
            for hh in range(FOX_HPS):
                acc = acc_s[hh]
                den = acc[HEAD_DIM:HEAD_DIM + 1, :]
                cols = slice(hh * HEAD_DIM, (hh + 1) * HEAD_DIM)
                o_ref[:, cols] = (acc[:HEAD_DIM, :] / den).T
                l_ref[:, cols] = jnp.broadcast_to(m_s[hh] + jnp.log(den), (HEAD_DIM, FOX_T)).T

    pairs = [(i, j) for i in range(nt) for j in range(i + 1)]
    i_tab = jnp.asarray([p[0] for p in pairs], jnp.int32)
    j_tab = jnp.asarray([p[1] for p in pairs], jnp.int32)
    qs = pl.BlockSpec((FOX_T, FOX_HPS * PACK), lambda hp, t, it, jt: (it[t], hp))
    ks = pl.BlockSpec((FOX_T, FOX_HPS * PACK), lambda hp, t, it, jt: (jt[t], hp))
    os_ = pl.BlockSpec((FOX_T, FOX_HPS * HEAD_DIM), lambda hp, t, it, jt: (it[t], hp))
    shp = jax.ShapeDtypeStruct((S, FOX_WIDTH), F32)
    grid_spec = pltpu.PrefetchScalarGridSpec(
        num_scalar_prefetch=2, grid=(N_FOX_HEADS // FOX_HPS, len(pairs)), in_specs=[qs, ks, ks], out_specs=[os_, os_],
        scratch_shapes=[pltpu.VMEM((FOX_HPS, 1, FOX_T), F32), pltpu.VMEM((FOX_HPS, PACK, FOX_T), F32)])
    return _pcall(body, name=name, grid_spec=grid_spec, out_shape=[shp, shp],
                  compiler_params=_params("parallel", "arbitrary"))(i_tab, j_tab, qp, kp, vp)


def _fox_pack_bwd(qkv, c, o, lse, do, *, name, tm=512, after=None):
    S = qkv.shape[0]

    def body(q_ref, c_ref, o_ref, l_ref, do_ref, qo_ref, do_out_ref):
        hp = pl.program_id(1)
        cv = c_ref[...]
        for hh in range(2):
            src = slice(hh * HEAD_DIM, (hh + 1) * HEAD_DIM)
            lo = slice(hh * PACK, hh * PACK + HEAD_DIM)
            hi = slice(hh * PACK + HEAD_DIM, (hh + 1) * PACK)
            shift = _head_column(cv, 2 * hp + hh) - l_ref[:, hh * HEAD_DIM:hh * HEAD_DIM + 1]
            dov = do_ref[:, src]
            dsum = jnp.sum(dov * o_ref[:, src], axis=-1, keepdims=True)
            qo_ref[:, lo] = (q_ref[:, src].astype(F32) * ATTN_SCALE).astype(qo_ref.dtype)
            qo_ref[:, hi] = _extras(_pieces(shift), ONES3, tm).astype(qo_ref.dtype)
            do_out_ref[:, lo] = dov.astype(do_out_ref.dtype)
            do_out_ref[:, hi] = _extras(_pieces(-dsum), ZEROS3, tm).astype(do_out_ref.dtype)

    pair = pl.BlockSpec((tm, PACK), lambda i, hp: (i, hp))
    out = pl.BlockSpec((tm, 2 * PACK), lambda i, hp: (i, hp))
    shp = jax.ShapeDtypeStruct((S, N_FOX_HEADS * PACK), _CD)
    return _pcall(body, after, name=name, grid=(S // tm, HEAD_PAIRS),
                  in_specs=[pl.BlockSpec((tm, PACK), lambda i, hp: (i, Q_BLOCK0 + hp)),
                            pl.BlockSpec((tm, PACK), lambda i, hp: (i, 0)), pair, pair, pair],
                  out_specs=[out, out], out_shape=[shp, shp],
                  compiler_params=_params("parallel", "parallel"))(qkv, c, o, lse, do)


def _fox_bwd(qp, kp, vp, dop, *, name):
    S = qp.shape[0]
    nt = S // FOX_T
    nt_dims = (((1,), (1,)), ((), ()))
    tn_dims = (((0,), (0,)), ((), ()))

    def body(i_tab, j_tab, q_ref, k_ref, v_ref, do_ref, dq_ref, dk_ref, dv_ref, dc_ref, dr_ref,
             dq_s, dk_s, dv_s, dc_s, dr_s):
        t = pl.program_id(1)
        i, j = i_tab[t], j_tab[t]

        @pl.when(t == 0)
        def _():
            dq_s[...] = jnp.zeros((S, FOX_HPS * PACK), F32)
            dr_s[...] = jnp.zeros((FOX_HPS, 1, S), F32)

        @pl.when(i == j)
        def _():
            dk_s[...] = jnp.zeros((FOX_T, FOX_HPS * PACK), F32)
            dv_s[...] = jnp.zeros((FOX_T, FOX_HPS * PACK), F32)
            dc_s[...] = jnp.zeros((FOX_HPS, FOX_T, 1), F32)

        def tile(diagonal):
            rows = pl.ds(pl.multiple_of(i * FOX_T, FOX_T), FOX_T)
            for hh in range(FOX_HPS):
                cols = slice(hh * PACK, (hh + 1) * PACK)
                qv, kv, vv, dov = q_ref[:, cols], k_ref[:, cols], v_ref[:, cols], do_ref[:, cols]
                pt = jnp.exp(lax.dot_general(kv, qv, nt_dims, preferred_element_type=F32))
                if diagonal:
                    key = lax.broadcasted_iota(jnp.int32, (FOX_T, FOX_T), 0)
                    qry = lax.broadcasted_iota(jnp.int32, (FOX_T, FOX_T), 1)
                    pt = jnp.where(key <= qry, pt, 0.0)
                dst = pt * lax.dot_general(vv, dov, nt_dims, preferred_element_type=F32)
                dsb = dst.astype(_CD)
                dc_s[hh] += jnp.sum(dst, axis=1, keepdims=True)
                dr_s[hh, :, rows] += jnp.sum(dst, axis=0, keepdims=True)
                dv_s[:, cols] += jnp.dot(pt.astype(_CD), dov, preferred_element_type=F32)
                dk_s[:, cols] += jnp.dot(dsb, qv, preferred_element_type=F32)
                dq_s[rows, cols] += lax.dot_general(dsb, kv, tn_dims, preferred_element_type=F32)

        @pl.when(i > j)
        def _():
            tile(False)

        @pl.when(i == j)
        def _():
            tile(True)

        @pl.when(i == nt - 1)
        def _():
            for hh in range(FOX_HPS):
                src = slice(hh * PACK, hh * PACK + HEAD_DIM)
                dst_cols = slice(hh * HEAD_DIM, (hh + 1) * HEAD_DIM)
                dk_ref[:, dst_cols] = dk_s[:, src].astype(dk_ref.dtype)
                dv_ref[:, dst_cols] = dv_s[:, src].astype(dv_ref.dtype)
                dc_ref[:, dst_cols] = jnp.broadcast_to(dc_s[hh], (FOX_T, HEAD_DIM))

        @pl.when(t == len(pairs) - 1)
        def _():
            for hh in range(FOX_HPS):
                dq_ref[:, hh * HEAD_DIM:(hh + 1) * HEAD_DIM] = (
                    dq_s[:, hh * PACK:hh * PACK + HEAD_DIM] * ATTN_SCALE).astype(dq_ref.dtype)
            dr_ref[...] = dr_s[...]

    pairs = [(i, j) for j in range(nt) for i in range(j, nt)]
    i_tab = jnp.asarray([p[0] for p in pairs], jnp.int32)
    j_tab = jnp.asarray([p[1] for p in pairs], jnp.int32)
    wide, narrow = FOX_HPS * PACK, FOX_HPS * HEAD_DIM
    qs = pl.BlockSpec((FOX_T, wide), lambda hp, t, it, jt: (it[t], hp))
    ks = pl.BlockSpec((FOX_T, wide), lambda hp, t, it, jt: (jt[t], hp))
    whole = pl.BlockSpec((S, narrow), lambda hp, t, it, jt: (0, hp))
    cs = pl.BlockSpec((FOX_T, narrow), lambda hp, t, it, jt: (jt[t], hp))
    rs = pl.BlockSpec((FOX_HPS, 1, S), lambda hp, t, it, jt: (hp, 0, 0))
    shp = jax.ShapeDtypeStruct((S, FOX_WIDTH), _CD)
    grid_spec = pltpu.PrefetchScalarGridSpec(
        num_scalar_prefetch=2, grid=(N_FOX_HEADS // FOX_HPS, len(pairs)), in_specs=[qs, ks, ks, qs],
        out_specs=[whole, cs, cs, cs, rs],
        scratch_shapes=[pltpu.VMEM((S, wide), F32), pltpu.VMEM((FOX_T, wide), F32),
                        pltpu.VMEM((FOX_T, wide), F32), pltpu.VMEM((FOX_HPS, FOX_T, 1), F32),
                        pltpu.VMEM((FOX_HPS, 1, S), F32)])
    return _pcall(body, name=name, grid_spec=grid_spec,
                  out_shape=[shp, shp, shp, jax.ShapeDtypeStruct((S, FOX_WIDTH), F32),
                             jax.ShapeDtypeStruct((N_FOX_HEADS, 1, S), F32)],
                  compiler_params=_params("parallel", "arbitrary"))(i_tab, j_tab, qp, kp, vp, dop)


def _layer_step(x, tgt, w, p, late_weights=None, grad_sink=None, after=None, first_weights=None):
    S = x.shape[0]
    after_norm, after_proj = after if after is not None else (None, None)
    h = _rms_fwd(x, p["norm_mix_g"], name="rms_mix", after=after_norm)
    if first_weights is not None:
        w = {**w, **first_weights(h)}
    qkv = _mm(h, w["qkv"][:, 3 * DIL_WIDTH:], name="proj_fox", out_dtype=_CD, tn=768, tm=2048, after=after_proj)
    dil_qkv = _proj_dil(h, w["qkv"], name="proj_dil")
    zf = _mm(h, w["f"], name="proj_f")
    gl = _mm(h, w["g"], name="proj_gate", tn=1024, out_dtype=_CD)

    dil_o, dil_l = [], []
    for g in range(N_DIL_GROUPS):
        og, lg = _dil_fwd(dil_qkv[g], g, name=f"dil_fwd{g}")
        dil_o.append(og), dil_l.append(lg)
    o_a = _dil_mix_fwd(dil_o, dil_l, name="dil_mix")

    c = _fox_cumsum(zf, p["b_fgt"], name="fox_cumsum")
    fqp, fkp, fvp = _fox_pack_fwd(qkv, c, name="fox_pack")
    o_b, flse = _fox_fwd(fqp, fkp, fvp, name="fox_fwd")

    if late_weights is not None:
        w = {**w, **late_weights(o_b)}
    y_a = _mm(o_a, w["dil_out"], name="y_a", tn=1024, out_dtype=_CD)
    y_b = _mm(o_b, w["fox_out"], name="y_b", tn=1024, out_dtype=_CD)
    merged = _gate_fwd(gl, p["b_gate"], y_a, y_b, name="gate_fwd")
    x1 = _mm(merged, w["out"], name="mix_out", add=x)

    h2 = _rms_fwd(x1, p["norm_ffn_g"], name="rms_ffn")
    gate, up, act = _ffn_in_act(h2, w["ffn_in"], name="ffn_in")
    x2 = _mm(act, w["ffn_down"], name="ffn_down", add=x1, tk=2816)

    loss, dx2, dg_final = _loss_head(x2, p["norm_final_g"], tgt, name="loss_head")

    gw_ffn_down = _mm(act, dx2, name="gw_ffn_down", ta=True, out_dtype=_CD, tm=1408)
    dgu = _d_swiglu(dx2, w["ffn_down"], gate, up, name="d_swiglu")
    dh2 = _mm(dgu, w["ffn_in"], name="d_h2", tb=True, tk=1408, b_blocks=True, tm=2048, a_halves=True)
    gw_ffn_in = _mm(h2, dgu, name="gw_ffn_in", ta=True, out_dtype=_CD, tn=1408, out_blocks=1408, b_halves=True)
    sink = grad_sink if grad_sink is not None else (lambda group, grads: None)
    tok = sink("ffn", dict(ffn_in=gw_ffn_in, ffn_down=gw_ffn_down))
    dx1, dg_ffn = _rms_bwd(x1, p["norm_ffn_g"], dh2, dx2, name="rms_ffn_bwd", after=tok)

    dmerged = _mm(dx1, w["out"], name="d_merged", tb=True, out_dtype=_CD)
    gw_out = _mm(merged, dx1, name="gw_out", ta=True, out_dtype=_CD)
    dy_a, dy_b, dgl, db_gate = _gate_bwd(dmerged, gl, p["b_gate"], y_a, y_b, name="gate_bwd")
    do_a = _mm(dy_a, w["dil_out"], name="d_o_a", tb=True)
    gw_dil_out = _mm(o_a, dy_a, name="gw_dil_out", ta=True, out_dtype=_CD, tn=1024)
    do_b = _mm(dy_b, w["fox_out"], name="d_o_b", tb=True)
    gw_fox_out = _mm(o_b, dy_b, name="gw_fox_out", ta=True, out_dtype=_CD, tn=1024)
    tok = sink("mix", dict(dil_out=gw_dil_out, fox_out=gw_fox_out, out=gw_out))

    bqp, bdop = _fox_pack_bwd(qkv, c, o_b, flse, do_b, name="fox_pack_bwd", after=tok)
    dqp, dkp, dvp, dck, dcq = _fox_bwd(bqp, fkp, fvp, bdop, name="fox_bwd")
    dc = dcq[:, 0, :].T - dck.reshape(S, N_FOX_HEADS, HEAD_DIM)[:, :, 0]
    dc = jnp.pad(dc, ((0, 0), (0, F_PAD - N_FOX_HEADS)))
    dzf, db_fgt = _fox_cumsum_bwd(dc, zf, p["b_fgt"], name="fox_cumsum_bwd")

    douts = _dil_mix_bwd(do_a, dil_o, dil_l, name="dil_mix_bwd", after=tok)
    dqs, dks, dvs = [], [], []
    for g in range(N_DIL_GROUPS):
        dq, dk, dv = _dil_bwd(dil_qkv[g], douts[3 + g], douts[g], g, name=f"dil_bwd{g}")
        for parts, t in ((dqs, dq), (dks, dk), (dvs, dv)):
            parts.extend([t[0].astype(_CD), t[1].astype(_CD)])
    dqkv = jnp.concatenate(dqs + dks + dvs + [dqp, dkp, dvp], axis=1)

    gw_qkv = _mm(h, dqkv, name="gw_qkv", ta=True, out_dtype=_CD, tn=768)
    gw_g = _mm(h, dgl, name="gw_gate", ta=True, out_dtype=_CD)
    gw_f = _mm(h, dzf, name="gw_f", ta=True, out_dtype=_CD)
    tok = sink("in", dict(qkv=gw_qkv, f=gw_f, g=gw_g))
    dh = _mm(dqkv, w["qkv"], name="d_h_qkv", tb=True, tk=1920, tm=2048, after=tok)
    dh = _mm(dgl, w["g"], name="d_h_gate", tb=True, add=dh)
    dh = _mm(dzf, w["f"], name="d_h_f", tb=True, add=dh)
    dx, dg_mix = _rms_bwd(x, p["norm_mix_g"], dh, dx1, name="rms_mix_bwd")

    gw = dict(qkv=gw_qkv, f=gw_f, g=gw_g, dil_out=gw_dil_out, fox_out=gw_fox_out, out=gw_out, ffn_in=gw_ffn_in,
              ffn_down=gw_ffn_down)
    small = dict(norm_mix_g=dg_mix, b_fgt=db_fgt, b_gate=db_gate, norm_ffn_g=dg_ffn, norm_final_g=dg_final)
    return loss, dx, gw, small


def _position():
    return lax.axis_index("x"), lax.axis_index("y"), lax.axis_index("c")


def _other_chips(x, y):
    return [(1 - x, y), (x, 1 - y), (1 - x, 1 - y)]


ROW_TILE = 16


def _row_chunks(rows, want=4):
    n = want
    while n > 1 and rows % (n * ROW_TILE):
        n //= 2
    return n


SEM_SPEC = pl.BlockSpec(memory_space=pltpu.SEMAPHORE)
ANY_SPEC = pl.BlockSpec(memory_space=pl.ANY)
DATAFLOW = pltpu.SideEffectType.DATAFLOW_SIDE_EFFECTING


def _in_hbm(a):
    return pltpu.with_memory_space_constraint(a, pltpu.HBM)


def _split_copy_start(srcs, land_shapes, copies, after, *, name):
    n, m = len(srcs), len(land_shapes)

    def body(*refs):
        src_refs, land_refs = refs[:n], refs[n:n + m]
        send_sems, recv_sems = refs[n + m + 1], refs[n + m + 2]
        token = refs[-1]
        x, y, c = _position()
        for k, (src, dst, peer) in enumerate(copies(x, y, c, src_refs, land_refs)):
            pltpu.make_async_remote_copy(src_ref=src, dst_ref=dst, send_sem=send_sems.at[k], recv_sem=recv_sems.at[k],
                                         device_id=peer, device_id_type=MESH).start()
        token[...] = jnp.zeros_like(token)

    lands = [lax.empty(s.shape, s.dtype) for s in land_shapes]
    count = len(copies(0, 0, 0, srcs, lands))
    out = _pcall(
        body, name=name,
        out_shape=(pltpu.SemaphoreType.DMA((count,)), pltpu.SemaphoreType.DMA((count,)),
                   *[pltpu.HBM(s.shape, s.dtype) for s in srcs], *[pltpu.HBM(s.shape, s.dtype) for s in land_shapes],
                   jax.ShapeDtypeStruct((8, 128), F32)),
        in_specs=[HBM_SPEC] * (n + m) + [ANY_SPEC],
        out_specs=(SEM_SPEC, SEM_SPEC, *[HBM_SPEC] * (n + m), pl.BlockSpec(memory_space=pltpu.VMEM)),
        input_output_aliases={k: 2 + k for k in range(n + m)},
        compiler_params=pltpu.CompilerParams(has_side_effects=DATAFLOW),
    )(*[_in_hbm(s) for s in srcs], *[_in_hbm(l) for l in lands], after)
    return out[0], out[1], list(out[2:2 + n]), list(out[2 + n:2 + n + m]), out[-1]


def _split_copy_wait(send_sems, recv_sems, srcs, lands, copies, after, *, name):
    n, m = len(srcs), len(lands)

    def body(*refs):
        src_refs, land_refs = refs[:n], refs[n:n + m]
        send, recv = refs[n + m], refs[n + m + 1]
        x, y, c = _position()
        for k, (src, dst, peer) in enumerate(copies(x, y, c, src_refs, land_refs)):
            cp = pltpu.make_async_remote_copy(src_ref=src, dst_ref=dst, send_sem=send.at[k], recv_sem=recv.at[k],
                                              device_id=peer, device_id_type=MESH)
            cp.wait_send()
            cp.wait_recv()

    afters = list(after) if isinstance(after, (list, tuple)) else [after]
    out = _pcall(
        body, name=name,
        out_shape=tuple(pltpu.HBM(s.shape, s.dtype) for s in list(srcs) + list(lands)),
        in_specs=[HBM_SPEC] * (n + m) + [SEM_SPEC, SEM_SPEC] + [ANY_SPEC] * len(afters),
        out_specs=tuple([HBM_SPEC] * (n + m)),
        input_output_aliases={k: k for k in range(n + m)},
        compiler_params=pltpu.CompilerParams(has_side_effects=DATAFLOW),
    )(*srcs, *lands, send_sems, recv_sems, *afters)
    return list(out[:n]), list(out[n:])


def _gather_copies(x, y, c, shard_refs, land_refs):
    out = []
    for s, l in zip(shard_refs, land_refs):
        half = s.shape[0] // 2
        nq = _row_chunks(half)
        for cx, cy in _other_chips(x, y):
            for q in range(nq):
                rows = pl.ds(c * half + q * (half // nq), half // nq)
                out.append((s.at[rows, :], l.at[2 * x + y, rows, :], (cx, cy, c)))
    return out


def _gather_whole_copies(x, y, c, shard_refs, land_refs):
    out = []
    for s, l in zip(shard_refs, land_refs):
        nq = _row_chunks(s.shape[0])
        for cx, cy in _other_chips(x, y):
            for q in range(nq):
                rows = pl.ds(q * (s.shape[0] // nq), s.shape[0] // nq)
                out.append((s.at[rows, :], l.at[2 * x + y, rows, :], (cx, cy, c)))
    return out


def _scatter_copies(x, y, c, part_refs, land_refs):
    out = []
    for p, l in zip(part_refs, land_refs):
        nq = _row_chunks(p.shape[1])
        for r, (cx, cy) in enumerate(_other_chips(x, y)):
            for q in range(nq):
                rows = pl.ds(q * (p.shape[1] // nq), p.shape[1] // nq)
                out.append((p.at[2 * cx + cy, rows, :], l.at[r, rows, :], (cx, cy, c)))
    return out


def _scatter_all_copies(x, y, c, block_refs, land_refs):
    out = []
    for g, l in zip(block_refs, land_refs):
        half = g.shape[1] // 2
        nq = _row_chunks(half)
        size = half // nq
        for q in range(nq):
            rows = pl.ds((1 - c) * half + q * size, size)
            out.append((g.at[2 * x + y, rows, :], l.at[0, pl.ds(q * size, size), :], (x, y, 1 - c)))
        for r, (cx, cy) in enumerate(_other_chips(x, y)):
            for j in range(2):
                h = c if j == 0 else 1 - c
                for q in range(nq):
                    rows = pl.ds(h * half + q * size, size)
                    out.append((g.at[2 * cx + cy, rows, :], l.at[1 + 2 * r + j, pl.ds(q * size, size), :], (cx, cy, h)))
    return out


def _forward_halves(lands, *, name):
    n = len(lands)

    def body(*refs):
        ins = refs[:n]
        send_sems, recv_sems = refs[2 * n:]
        x, y, c = _position()
        copies = []
        for w in range(n):
            half = ins[w].shape[1] // 2
            for r, (cx, cy) in enumerate(_other_chips(x, y)):
                blk = ins[w].at[2 * cx + cy, pl.ds(c * half, half), :]
                cp = pltpu.make_async_remote_copy(src_ref=blk, dst_ref=blk, send_sem=send_sems.at[w, r],
                                                  recv_sem=recv_sems.at[w, r], device_id=(x, y, 1 - c),
                                                  device_id_type=MESH)
                cp.start()
                copies.append(cp)
        for w in range(n):
            half = ins[w].shape[1] // 2
            for r, (cx, cy) in enumerate(_other_chips(x, y)):
                blk = ins[w].at[2 * cx + cy, pl.ds((1 - c) * half, half), :]
                pltpu.make_async_remote_copy(src_ref=blk, dst_ref=blk, send_sem=send_sems.at[w, r],
                                             recv_sem=recv_sems.at[w, r], device_id=(x, y, 1 - c),
                                             device_id_type=MESH).wait_recv()
        for cp in copies:
            cp.wait_send()

    return _pcall(
        body, name=name, in_specs=[HBM_SPEC] * n, out_specs=[HBM_SPEC] * n,
        out_shape=[jax.ShapeDtypeStruct(l.shape, l.dtype) for l in lands],
        input_output_aliases={k: k for k in range(n)},
        scratch_shapes=[pltpu.SemaphoreType.DMA((n, 3)), pltpu.SemaphoreType.DMA((n, 3))],
    )(*lands)


def _swap_halves(grads, name="swap_halves"):
    n = len(grads)

    def body(*refs):
        ins, outs = refs[:n], refs[n:2 * n]
        send_sems, recv_sems = refs[2 * n:]
        x, y, c = _position()
        copies = []
        for w in range(n):
            half = ins[w].shape[1] // 2
            cp = pltpu.make_async_remote_copy(
                src_ref=ins[w].at[:, pl.ds((1 - c) * half, half), :], dst_ref=outs[w], send_sem=send_sems.at[w],
                recv_sem=recv_sems.at[w], device_id=(x, y, 1 - c), device_id_type=MESH)
            cp.start()
            copies.append(cp)
        for cp in copies:
            cp.wait()

    return _pcall(
        body, name=name, in_specs=[HBM_SPEC] * n, out_specs=[HBM_SPEC] * n,
        out_shape=[jax.ShapeDtypeStruct((4, g.shape[1] // 2, g.shape[2]), g.dtype) for g in grads],
        scratch_shapes=[pltpu.SemaphoreType.DMA((n,)), pltpu.SemaphoreType.DMA((n,))],
    )(*grads)


def _share_halves(halves):
    n = len(halves)

    def body(*refs):
        ins, outs = refs[:n], refs[n:2 * n]
        send_sems, recv_sems = refs[2 * n:]
        x, y, c = _position()
        copies = []
        for w in range(n):
            cp = pltpu.make_async_remote_copy(src_ref=ins[w], dst_ref=outs[w], send_sem=send_sems.at[w],
                                              recv_sem=recv_sems.at[w], device_id=(x, y, 1 - c), device_id_type=MESH)
            cp.start()
            copies.append(cp)
        for cp in copies:
            cp.wait()

    return _pcall(
        body, name="share_halves", in_specs=[HBM_SPEC] * n, out_specs=[HBM_SPEC] * n,
        out_shape=[jax.ShapeDtypeStruct(h.shape, h.dtype) for h in halves],
        scratch_shapes=[pltpu.SemaphoreType.DMA((n,)), pltpu.SemaphoreType.DMA((n,))],
    )(*halves)


def _sum_small(part):
    rows, width = part.shape

    def body(x_ref, out_ref, all_ref, send_sems, recv_sems):
        x, y, c = _position()
        me, sibling = (x, y, c), (x, y, 1 - c)
        chips = _other_chips(x, y)

        def block(px, py, pc):
            return all_ref.at[pl.ds((4 * px + 2 * py + pc) * rows, rows), :]

        def copy(k, blk, to, src=None):
            return pltpu.make_async_remote_copy(
                src_ref=block(*blk) if src is None else src, dst_ref=block(*blk), send_sem=send_sems.at[k],
                recv_sem=recv_sems.at[k], device_id=to, device_id_type=MESH)

        all_ref[pl.ds((4 * x + 2 * y + c) * rows, rows), :] = x_ref[...]
        first = [copy(0, me, sibling, src=x_ref)]
        first += [copy(1 + j, me, (*chip, c), src=x_ref) for j, chip in enumerate(chips)]
        for cp in first:
            cp.start()
        passed = [copy(4 + j, (*chip, c), sibling) for j, chip in enumerate(chips)]
        for j, chip in enumerate(chips):
            copy(1 + j, (*chip, c), me).wait_recv()
            passed[j].start()
        copy(0, sibling, me).wait_recv()
        for j, chip in enumerate(chips):
            copy(4 + j, (*chip, 1 - c), me).wait_recv()
        for cp in first + passed:
            cp.wait_send()
        total = all_ref[0:rows, :]
        for d in range(1, 8):
            total = total + all_ref[d * rows:(d + 1) * rows, :]
        out_ref[...] = total

    vm = pl.BlockSpec(memory_space=pltpu.VMEM)
    return _pcall(
        body, name="sum_small", in_specs=[vm], out_specs=vm, out_shape=jax.ShapeDtypeStruct((rows, width), F32),
        scratch_shapes=[pltpu.VMEM((8 * rows, width), F32), pltpu.SemaphoreType.DMA((7,)), pltpu.SemaphoreType.DMA((7,))],
    )(part)


def _row_tile(R, C, itemsize=4, budget=1 << 20):
    for t in (512, 256, 128, 64, 32, 16, 8):
        if R % t == 0 and t * C * itemsize <= budget:
            return t
    return R


def _add_halves(g, recv, c, *, name):
    _, R, C = g.shape
    half = R // 2
    t = _row_tile(half, C)
    nb = half // t

    def body(c_ref, g_ref, r_ref, o_ref):
        o_ref[...] = (g_ref[...].astype(F32) + r_ref[...].astype(F32)).astype(o_ref.dtype)

    grid_spec = pltpu.PrefetchScalarGridSpec(
        num_scalar_prefetch=1, grid=(4, nb),
        in_specs=[pl.BlockSpec((1, t, C), lambda k, i, cr: (k, cr[0] * nb + i, 0)),
                  pl.BlockSpec((1, t, C), lambda k, i, cr: (k, i, 0))],
        out_specs=pl.BlockSpec((1, t, C), lambda k, i, cr: (k, i, 0)))
    return _pcall(body, name=name, grid_spec=grid_spec, out_shape=jax.ShapeDtypeStruct((4, half, C), g.dtype),
                  compiler_params=_params("parallel", "parallel"))(c, g, recv)


def _add_all(g, recv, where, *, name):
    _, R, C = g.shape
    half = R // 2
    t = _row_tile(half, C)
    nb = half // t

    def body(w_ref, g_ref, r_ref, o_ref):
        total = g_ref[0].astype(F32)
        for k in range(7):
            total = total + r_ref[k].astype(F32)
        o_ref[...] = total

    grid_spec = pltpu.PrefetchScalarGridSpec(
        num_scalar_prefetch=1, grid=(nb,),
        in_specs=[pl.BlockSpec((1, t, C), lambda i, wr: (wr[0], wr[1] * nb + i, 0)),
                  pl.BlockSpec((7, t, C), lambda i, wr: (0, i, 0))],
        out_specs=pl.BlockSpec((t, C), lambda i, wr: (i, 0)))
    return _pcall(body, name=name, grid_spec=grid_spec, out_shape=jax.ShapeDtypeStruct((half, C), F32),
                  compiler_params=_params("parallel"))(where, g, recv)


def _add_owners(mine, recv, *, name):
    half, C = mine.shape
    t = _row_tile(half, C)

    def body(m_ref, r_ref, o_ref):
        o_ref[...] = ((m_ref[...].astype(F32) + r_ref[0].astype(F32)) + r_ref[1].astype(F32)) + r_ref[2].astype(F32)

    return _pcall(body, name=name, grid=(half // t,),
                  in_specs=[pl.BlockSpec((t, C), lambda i: (i, 0)), pl.BlockSpec((3, t, C), lambda i: (0, i, 0))],
                  out_specs=pl.BlockSpec((t, C), lambda i: (i, 0)), out_shape=jax.ShapeDtypeStruct((half, C), F32),
                  compiler_params=_params("parallel"))(mine, recv)


def _adamw(w, g, m, v, *, name):
    R, C = w.shape
    t = _row_tile(R, C)
    c1 = 1.0 - ADAM_B1 ** ADAM_STEP
    c2 = 1.0 - ADAM_B2 ** ADAM_STEP

    def body(w_ref, g_ref, m_ref, v_ref, d_ref, nm_ref, nv_ref):
        gv = g_ref[...]
        mn = ADAM_B1 * m_ref[...] + (1.0 - ADAM_B1) * gv
        vn = ADAM_B2 * v_ref[...] + (1.0 - ADAM_B2) * (gv * gv)
        d_ref[...] = -ADAM_LR * ((mn / c1) / (jnp.sqrt(vn / c2) + ADAM_EPS) + ADAM_WD * w_ref[...])
        nm_ref[...] = mn
        nv_ref[...] = vn

    blk = pl.BlockSpec((t, C), lambda i: (i, 0))
    shp = jax.ShapeDtypeStruct((R, C), F32)
    return _pcall(body, name=name, grid=(R // t,), in_specs=[blk] * 4, out_specs=[blk] * 3, out_shape=[shp] * 3,
                  compiler_params=_params("parallel"))(w, g, m, v)


BIG = ("w_in", "w_dil_out", "w_fox_out", "w_out", "w_ffn_in", "w_ffn_down")
SMALL = ("norm_mix_g", "b_fgt", "b_gate", "norm_ffn_g", "norm_final_g")
ORDER = ("norm_mix_g", "w_in", "b_fgt", "b_gate", "w_dil_out", "w_fox_out", "w_out", "norm_ffn_g", "w_ffn_in",
         "w_ffn_down", "norm_final_g")
SMALL_ROWS = {"norm_mix_g": (0, 1), "b_gate": (1, 3), "norm_ffn_g": (3, 4), "norm_final_g": (4, 5), "b_fgt": (5, 6)}


def _columns_to_blocks(full, ncol):
    K = full.shape[0]
    return full.reshape(K, 4, ncol).transpose(1, 0, 2)


def _blocks_to_columns(blocks):
    n, K, ncol = blocks.shape
    return blocks.transpose(1, 0, 2).reshape(K, n * ncol)


def kernel(x, norm_mix_g, w_in, b_fgt, b_gate, w_dil_out, w_fox_out, w_out, norm_ffn_g, w_ffn_in, w_ffn_down, norm_final_g, loss_target, m_norm_mix_g, m_w_in, m_b_fgt, m_b_gate, m_w_dil_out, m_w_fox_out, m_w_out, m_norm_ffn_g, m_w_ffn_in, m_w_ffn_down, m_norm_final_g, v_norm_mix_g, v_w_in, v_b_fgt, v_b_gate, v_w_dil_out, v_w_fox_out, v_w_out, v_norm_ffn_g, v_w_ffn_in, v_w_ffn_down, v_norm_final_g):
    weights = dict(norm_mix_g=norm_mix_g, w_in=w_in, b_fgt=b_fgt, b_gate=b_gate, w_dil_out=w_dil_out,
                   w_fox_out=w_fox_out, w_out=w_out, norm_ffn_g=norm_ffn_g, w_ffn_in=w_ffn_in, w_ffn_down=w_ffn_down,
                   norm_final_g=norm_final_g)
    m_in = dict(norm_mix_g=m_norm_mix_g, w_in=m_w_in, b_fgt=m_b_fgt, b_gate=m_b_gate, w_dil_out=m_w_dil_out,
                w_fox_out=m_w_fox_out, w_out=m_w_out, norm_ffn_g=m_norm_ffn_g, w_ffn_in=m_w_ffn_in,
                w_ffn_down=m_w_ffn_down, norm_final_g=m_norm_final_g)
    v_in = dict(norm_mix_g=v_norm_mix_g, w_in=v_w_in, b_fgt=v_b_fgt, b_gate=v_b_gate, w_dil_out=v_w_dil_out,
                w_fox_out=v_w_fox_out, w_out=v_w_out, norm_ffn_g=v_norm_ffn_g, w_ffn_in=v_w_ffn_in,
                w_ffn_down=v_w_ffn_down, norm_final_g=v_norm_final_g)
    c = lax.axis_index("c")
    chip = 2 * lax.axis_index("x") + lax.axis_index("y")

    shards = {n: weights[n][0].astype(_CD) for n in BIG}
    in_shape = jax.ShapeDtypeStruct((4,) + shards["w_in"].shape, _CD)
    send_i, recv_i, in_src, in_land, token_in = _split_copy_start(
        [shards["w_in"]], [in_shape], _gather_copies, norm_mix_g, name="gather_in_start")
    late = BIG[1:]
    send_g, recv_g, late_src, late_land, token = _split_copy_start(
        [shards[n] for n in late], [jax.ShapeDtypeStruct((4,) + shards[n].shape, _CD) for n in late],
        _gather_whole_copies, token_in, name="gather_late_start")
    adam_in = [t[0] + token_in[0, 0] for t in (w_in, m_w_in, v_w_in)]
    p = dict(norm_mix_g=norm_mix_g, b_fgt=jnp.pad(b_fgt, ((0, 0), (0, F_PAD - N_FOX_HEADS))), b_gate=b_gate,
             norm_ffn_g=norm_ffn_g, norm_final_g=norm_final_g.reshape(1, D_MODEL))

    def first_weights(after):
        own, lands = _split_copy_wait(send_i, recv_i, in_src, in_land, _gather_copies, [after] + adam_in,
                                      name="gather_in_wait")
        (g_in,) = _forward_halves(lands, name="gather_in_forward")
        full_in = _blocks_to_columns(lax.dynamic_update_index_in_dim(g_in, own[0], chip, 0))
        o3 = QKV_COLS
        o4 = o3 + N_FOX_HEADS
        return dict(qkv=full_in[:, :o3], f=jnp.pad(full_in[:, o3:o4], ((0, 0), (0, F_PAD - N_FOX_HEADS))),
                    g=full_in[:, o4:])

    def late_weights(after):
        own, lands = _split_copy_wait(send_g, recv_g, late_src, late_land, _gather_whole_copies, after,
                                      name="gather_late_wait")
        g_dil, g_fox, g_out, g_ffn_in, g_ffn_down = [
            lax.dynamic_update_index_in_dim(l, s, chip, 0) for l, s in zip(lands, own)]
        return dict(dil_out=_blocks_to_columns(g_dil), fox_out=_blocks_to_columns(g_fox),
                    out=g_out.reshape(D_MODEL, D_MODEL), ffn_in=g_ffn_in,
                    ffn_down=g_ffn_down.reshape(D_FF, D_MODEL))

    c_arr = jnp.reshape(c, (1,)).astype(jnp.int32)

    def to_blocks(n, full):
        shape = weights[n].shape
        if full.ndim == 3:
            return full
        if n in ("w_out", "w_ffn_down"):
            return full.reshape(4, shape[1], shape[2])
        return _columns_to_blocks(full, shape[2])

    def pair_sums(group, named):
        names = list(named)
        blocks = [to_blocks(n, named[n]) for n in names]
        from_sibling = _swap_halves(blocks, name=f"swap_halves_{group}")
        return [_add_halves(b, r, c_arr, name=f"add_halves_{n}") for b, r, n in zip(blocks, from_sibling, names)]

    in_flight = {}

    def grad_sink(group, gw):
        if group == "in":
            named = {"w_in": jnp.concatenate([gw["qkv"], gw["f"][:, :N_FOX_HEADS], gw["g"]], axis=1)}
        else:
            named = {"w_" + k: v for k, v in gw.items()}
        after = next(iter(gw.values()))
        if group == "in":
            srcs = pair_sums(group, named)
            lands = [jax.ShapeDtypeStruct((3,) + s.shape[1:], s.dtype) for s in srcs]
            copies = _scatter_copies
        else:
            srcs = [to_blocks(n, named[n]) for n in named]
            lands = [jax.ShapeDtypeStruct((7, s.shape[1] // 2, s.shape[2]), s.dtype) for s in srcs]
            copies = _scatter_all_copies
        started = _split_copy_start(srcs, lands, copies, after, name=f"scatter_{group}_start")
        in_flight[group] = (list(named), copies, started)
        return started[-1]

    loss_part, grad_x, gw, small = _layer_step(x[0], loss_target[0], {}, p, late_weights, grad_sink,
                                               (token_in, token), first_weights)

    def owner_sums(names, sums, from_chips):
        return {n: _add_owners(lax.dynamic_index_in_dim(s, chip, 0, keepdims=False), r, name=f"add_owners_{n}")
                for n, s, r in zip(names, sums, from_chips)}

    halves = {}
    where = jnp.stack([chip, c]).astype(jnp.int32)
    for group, (names, copies, (send_s, recv_s, srcs, lands, _)) in in_flight.items():
        srcs, recv = _split_copy_wait(send_s, recv_s, srcs, lands, copies, grad_x, name=f"scatter_{group}_wait")
        if copies is _scatter_copies:
            halves.update(owner_sums(names, srcs, recv))
        else:
            halves.update({n: _add_all(s, r, where, name=f"add_all_{n}") for n, s, r in zip(names, srcs, recv)})
    halves = [halves[n] for n in BIG]
    grads = {}
    for n, own, other in zip(BIG, halves, _share_halves(halves)):
        pair = jnp.stack([own, other])
        grads[n] = jnp.where(c == 0, pair, pair[::-1]).reshape(2 * own.shape[0], own.shape[1])

    packed = jnp.concatenate([
        small["norm_mix_g"], small["b_gate"].reshape(2, D_MODEL), small["norm_ffn_g"], small["norm_final_g"],
        jnp.pad(small["b_fgt"], ((0, 0), (0, D_MODEL - F_PAD))), jnp.pad(loss_part, ((0, 0), (0, D_MODEL - 1))),
        jnp.zeros((1, D_MODEL), F32)], axis=0)
    summed = _sum_small(packed)
    for n in SMALL:
        lo, hi = SMALL_ROWS[n]
        grads[n] = summed[lo:hi].reshape(1, -1)[:, :weights[n].size]
    loss = summed[6, 0]

    out_g, out_d, out_m, out_v = {}, {}, {}, {}
    for n in ORDER:
        shape = weights[n].shape
        two_d = shape[1:] if len(shape) == 3 else (1, weights[n].size)
        g2 = grads[n].reshape(two_d)
        wmv = adam_in if n == "w_in" else [t.reshape(two_d) for t in (weights[n], m_in[n], v_in[n])]
        d2, m2, v2 = _adamw(wmv[0], g2, wmv[1], wmv[2], name=f"adamw_{n}")
        out_g[n], out_d[n], out_m[n], out_v[n] = (g2.reshape(shape), d2.reshape(shape), m2.reshape(shape),
                                                  v2.reshape(shape))
    return (loss, grad_x[None], *[out_g[n] for n in ORDER], *[out_d[n] for n in ORDER],
            *[out_m[n] for n in ORDER], *[out_v[n] for n in ORDER])
```

```python
import numpy as np
import jax
import jax.numpy as jnp
from jax import lax
from jax.experimental import pallas as pl
from jax.experimental.pallas import tpu as pltpu

F32 = jnp.float32
_CD = jnp.bfloat16

D_MODEL = 1024
HEAD_DIM = 64
DIL_PAIRS = ((128, 1), (512, 4), (2048, 16))
N_DIL_GROUPS = 3
DIL_HEADS = 4
DIL_W = 128
DIL_OUT = DIL_HEADS * HEAD_DIM
DIL_WIDTH = N_DIL_GROUPS * DIL_OUT
N_FOX_HEADS = 8
FOX_WIDTH = N_FOX_HEADS * HEAD_DIM
D_FF = 2816
QKV_COLS = 3 * DIL_WIDTH + 3 * FOX_WIDTH
F_PAD = 128
RMS_EPS = 1e-6
NEG_INF = -1e30
ATTN_SCALE = HEAD_DIM ** -0.5
ADAM_LR, ADAM_B1, ADAM_B2, ADAM_EPS, ADAM_WD, ADAM_STEP = 0.001, 0.9, 0.999, 1e-08, 0.01, 10

VMEM_LIMIT = 48 * 1024 * 1024
VMEM_LIMIT_RESIDENT = 56 * 1024 * 1024
LANES = 128
MESH = pl.DeviceIdType.MESH
HBM_SPEC = pl.BlockSpec(memory_space=pltpu.HBM)


def _pcall(body, after=None, **kw):
    if after is None:
        return pl.pallas_call(body, **kw)
    n_in = len(kw["in_specs"])
    kw["in_specs"] = list(kw["in_specs"]) + [pl.BlockSpec(memory_space=pl.ANY)]

    def tied(*refs):
        return body(*refs[:n_in], *refs[n_in + 1:])

    call = pl.pallas_call(tied, **kw)
    return lambda *args: call(*args, after)


def _params(*sem):
    return pltpu.CompilerParams(dimension_semantics=sem, vmem_limit_bytes=VMEM_LIMIT)


def _pick(dim, pref):
    t = (min(pref, dim) // 128) * 128
    while t >= 128:
        if dim % t == 0:
            return t
        t -= 128
    return dim


def _mm(a, b, *, name, ta=False, tb=False, out_dtype=F32, add=None, tm=1024, tn=512, tk=2048, after=None,
        b_blocks=False, out_blocks=None, a_halves=False, b_halves=False):
    if a_halves:
        M, K = a.shape[1], 2 * a.shape[2]
    elif ta:
        K, M = a.shape
    else:
        M, K = a.shape
    if b_halves:
        b_rows, b_cols = b.shape[1], 2 * b.shape[2]
    else:
        b_rows, b_cols = (b.shape[1], b.shape[0] * b.shape[2]) if b_blocks else b.shape
    if tb:
        N, K2 = b_rows, b_cols
    else:
        K2, N = b_rows, b_cols
    assert K == K2, (a.shape, b.shape)
    shard = b.shape[2] if b_blocks else None
    tm = _pick(M, tm)
    tn = _pick(shard if (b_blocks and not tb) else (out_blocks or N), tn)
    tk = _pick(shard if (b_blocks and tb) else K, tk)
    nk = K // tk
    dn = (((0 if ta else 1,), (1 if tb else 0,)), ((), ()))
    has_add = add is not None
    assert not (has_add and out_blocks)

    def body(*refs):
        a_ref, b_ref = refs[0], refs[1]
        add_ref = refs[2] if has_add else None
        o_ref = refs[3] if has_add else refs[2]
        bv = b_ref[0] if b_blocks else b_ref[...]
        p = lax.dot_general(a_ref[...].astype(_CD), bv.astype(_CD), dn, preferred_element_type=F32)

        def finish(r):
            if has_add:
                r = r + add_ref[...]
            if out_blocks:
                o_ref[0] = r.astype(out_dtype)
            else:
                o_ref[...] = r.astype(out_dtype)

        if nk == 1:
            finish(p)
        else:
            acc_ref = refs[-1]
            k = pl.program_id(2)

            @pl.when(k == 0)
            def _():
                acc_ref[...] = p

            @pl.when(k > 0)
            def _():
                acc_ref[...] += p

            @pl.when(k == nk - 1)
            def _():
                finish(acc_ref[...])

    if a_halves:
        ka = (K // 2) // tk
        a_spec = pl.BlockSpec((None, tm, tk), lambda i, j, k: (k // ka, i, k % ka))
    else:
        a_spec = pl.BlockSpec((tk, tm), lambda i, j, k: (k, i)) if ta else pl.BlockSpec((tm, tk), lambda i, j, k: (i, k))
    if b_halves:
        nb_ = (N // 2) // tn
        b_spec = pl.BlockSpec((None, tk, tn), lambda i, j, k: (j // nb_, k, j % nb_))
    elif b_blocks and tb:
        per = shard // tk
        b_spec = pl.BlockSpec((1, tn, tk), lambda i, j, k: (k // per, j, k % per))
    elif b_blocks:
        per = shard // tn
        b_spec = pl.BlockSpec((1, tk, tn), lambda i, j, k: (j // per, k, j % per))
    else:
        b_spec = pl.BlockSpec((tn, tk), lambda i, j, k: (j, k)) if tb else pl.BlockSpec((tk, tn), lambda i, j, k: (k, j))
    if out_blocks:
        oper = out_blocks // tn
        o_spec = pl.BlockSpec((1, tm, tn), lambda i, j, k: (j // oper, i, j % oper))
        out_shape = jax.ShapeDtypeStruct((N // out_blocks, M, out_blocks), out_dtype)
    else:
        o_spec = pl.BlockSpec((tm, tn), lambda i, j, k: (i, j))
        out_shape = jax.ShapeDtypeStruct((M, N), out_dtype)
    in_specs = [a_spec, b_spec] + ([o_spec] if has_add else [])
    args = (a, b) + ((add,) if has_add else ())
    return _pcall(
        body, after, name=name, grid=(M // tm, N // tn, nk), in_specs=in_specs, out_specs=o_spec,
        out_shape=out_shape,
        scratch_shapes=[pltpu.VMEM((tm, tn), F32)] if nk > 1 else [],
        compiler_params=_params("parallel", "parallel", "arbitrary"),
    )(*args)


def _rms_fwd(x, g, *, name, tm=512, after=None):
    S, D = x.shape

    def body(x_ref, g_ref, h_ref):
        xv = x_ref[...]
        r = lax.rsqrt(jnp.mean(xv * xv, axis=-1, keepdims=True) + RMS_EPS)
        h_ref[...] = ((xv * r) * g_ref[...]).astype(h_ref.dtype)

    row = pl.BlockSpec((tm, D), lambda i: (i, 0))
    return _pcall(body, after, name=name, grid=(S // tm,), in_specs=[row, pl.BlockSpec((1, D), lambda i: (0, 0))],
                  out_specs=row, out_shape=jax.ShapeDtypeStruct((S, D), _CD), compiler_params=_params("parallel"))(x, g)


def _rms_bwd(x, g, dh, dres, *, name, tm=512, after=None):
    S, D = x.shape

    def body(x_ref, g_ref, dh_ref, dres_ref, dx_ref, dg_ref):
        xv = x_ref[...]
        r = lax.rsqrt(jnp.mean(xv * xv, axis=-1, keepdims=True) + RMS_EPS)
        xh = xv * r
        dhv = dh_ref[...]
        dxh = dhv * g_ref[...]
        dx_ref[...] = dres_ref[...] + r * (dxh - xh * jnp.mean(dxh * xh, axis=-1, keepdims=True))
        part = jnp.sum(dhv * xh, axis=0, keepdims=True)

        @pl.when(pl.program_id(0) == 0)
        def _():
            dg_ref[...] = part

        @pl.when(pl.program_id(0) > 0)
        def _():
            dg_ref[...] += part

    row = pl.BlockSpec((tm, D), lambda i: (i, 0))
    vec = pl.BlockSpec((1, D), lambda i: (0, 0))
    return _pcall(body, after, name=name, grid=(S // tm,), in_specs=[row, vec, row, row], out_specs=[row, vec],
                  out_shape=[jax.ShapeDtypeStruct((S, D), F32), jax.ShapeDtypeStruct((1, D), F32)],
                  compiler_params=_params("arbitrary"))(x, g, dh, dres)


def _loss_head(x, g, tgt, *, name, tm=512):
    S, D = x.shape

    def body(x_ref, g_ref, t_ref, loss_ref, dx_ref, dg_ref):
        xv = x_ref[...]
        gv = g_ref[...]
        r = lax.rsqrt(jnp.mean(xv * xv, axis=-1, keepdims=True) + RMS_EPS)
        xh = xv * r
        err = xh * gv - t_ref[...]
        lpart = 0.5 * jnp.sum(jnp.mean(err * err, axis=-1, keepdims=True), axis=0, keepdims=True)
        dy = err * (1.0 / D)
        dxh = dy * gv
        dx_ref[...] = r * (dxh - xh * jnp.mean(dxh * xh, axis=-1, keepdims=True))
        gpart = jnp.sum(dy * xh, axis=0, keepdims=True)

        @pl.when(pl.program_id(0) == 0)
        def _():
            loss_ref[...] = lpart
            dg_ref[...] = gpart

        @pl.when(pl.program_id(0) > 0)
        def _():
            loss_ref[...] += lpart
            dg_ref[...] += gpart

    row = pl.BlockSpec((tm, D), lambda i: (i, 0))
    vec = pl.BlockSpec((1, D), lambda i: (0, 0))
    one = pl.BlockSpec((1, 1), lambda i: (0, 0))
    return _pcall(body, name=name, grid=(S // tm,), in_specs=[row, vec, row], out_specs=[one, row, vec],
                  out_shape=[jax.ShapeDtypeStruct((1, 1), F32), jax.ShapeDtypeStruct((S, D), F32),
                             jax.ShapeDtypeStruct((1, D), F32)],
                  compiler_params=_params("arbitrary"))(x, g, tgt)


def _sigmoid(z):
    return 1.0 / (1.0 + jnp.exp(-z))


def _gate_fwd(gl, bg, ya, yb, *, name, tm=512):
    S, D = ya.shape

    def body(za_ref, zb_ref, ba_ref, bb_ref, ya_ref, yb_ref, o_ref):
        ga = _sigmoid(za_ref[...].astype(F32) + ba_ref[...])
        gb = _sigmoid(zb_ref[...].astype(F32) + bb_ref[...])
        o_ref[...] = (ga * ya_ref[...].astype(F32) + gb * yb_ref[...].astype(F32)).astype(o_ref.dtype)

    lo = pl.BlockSpec((tm, D), lambda i: (i, 0))
    hi = pl.BlockSpec((tm, D), lambda i: (i, 1))
    vlo = pl.BlockSpec((1, D), lambda i: (0, 0))
    vhi = pl.BlockSpec((1, D), lambda i: (0, 1))
    return _pcall(body, name=name, grid=(S // tm,), in_specs=[lo, hi, vlo, vhi, lo, lo], out_specs=lo,
                  out_shape=jax.ShapeDtypeStruct((S, D), _CD), compiler_params=_params("parallel"))(gl, gl, bg, bg, ya, yb)


def _gate_bwd(dm, gl, bg, ya, yb, *, name, tm=256):
    S, D = ya.shape

    def body(dm_ref, za_ref, zb_ref, ba_ref, bb_ref, ya_ref, yb_ref, dya_ref, dyb_ref, dgl_ref, dbg_ref):
        dmv = dm_ref[...].astype(F32)
        ga = _sigmoid(za_ref[...].astype(F32) + ba_ref[...])
        gb = _sigmoid(zb_ref[...].astype(F32) + bb_ref[...])
        dya_ref[...] = (dmv * ga).astype(dya_ref.dtype)
        dyb_ref[...] = (dmv * gb).astype(dyb_ref.dtype)
        dza = dmv * ya_ref[...].astype(F32) * ga * (1.0 - ga)
        dzb = dmv * yb_ref[...].astype(F32) * gb * (1.0 - gb)
        dgl_ref[:, :D] = dza.astype(dgl_ref.dtype)
        dgl_ref[:, D:] = dzb.astype(dgl_ref.dtype)
        pa = jnp.sum(dza, axis=0, keepdims=True)
        pb = jnp.sum(dzb, axis=0, keepdims=True)

        @pl.when(pl.program_id(0) == 0)
        def _():
            dbg_ref[:, :D] = pa
            dbg_ref[:, D:] = pb

        @pl.when(pl.program_id(0) > 0)
        def _():
            dbg_ref[:, :D] += pa
            dbg_ref[:, D:] += pb

    lo = pl.BlockSpec((tm, D), lambda i: (i, 0))
    hi = pl.BlockSpec((tm, D), lambda i: (i, 1))
    vlo = pl.BlockSpec((1, D), lambda i: (0, 0))
    vhi = pl.BlockSpec((1, D), lambda i: (0, 1))
    wide = pl.BlockSpec((tm, 2 * D), lambda i: (i, 0))
    vwide = pl.BlockSpec((1, 2 * D), lambda i: (0, 0))
    return _pcall(body, name=name, grid=(S // tm,), in_specs=[lo, lo, hi, vlo, vhi, lo, lo],
                  out_specs=[lo, lo, wide, vwide],
                  out_shape=[jax.ShapeDtypeStruct((S, D), _CD), jax.ShapeDtypeStruct((S, D), _CD),
                             jax.ShapeDtypeStruct((S, 2 * D), _CD), jax.ShapeDtypeStruct((1, 2 * D), F32)],
                  compiler_params=_params("arbitrary"))(dm, gl, gl, bg, bg, ya, yb)


def _ffn_in_act(h2, w_blocks, *, name, tm=512):
    S, D = h2.shape
    _, _, C = w_blocks.shape

    def body(a_ref, bg_ref, bu_ref, g_ref, u_ref, o_ref):
        av = a_ref[...].astype(_CD)
        gv = jnp.dot(av, bg_ref[0].astype(_CD), preferred_element_type=F32)
        uv = jnp.dot(av, bu_ref[0].astype(_CD), preferred_element_type=F32)
        g_ref[...] = gv.astype(g_ref.dtype)
        u_ref[...] = uv.astype(u_ref.dtype)
        o_ref[...] = (gv * _sigmoid(gv) * uv).astype(o_ref.dtype)

    out = pl.BlockSpec((tm, C), lambda i, j: (i, j))
    shp = jax.ShapeDtypeStruct((S, 2 * C), _CD)
    return _pcall(body, name=name, grid=(S // tm, 2),
                  in_specs=[pl.BlockSpec((tm, D), lambda i, j: (i, 0)), pl.BlockSpec((1, D, C), lambda i, j: (j, 0, 0)),
                            pl.BlockSpec((1, D, C), lambda i, j: (2 + j, 0, 0))],
                  out_specs=[out, out, out], out_shape=[shp, shp, shp],
                  compiler_params=_params("parallel", "arbitrary"))(h2, w_blocks, w_blocks)


def _d_swiglu(dx, w_down, gate, up, *, name, tm=512, tn=1408):
    S, D = dx.shape
    F = w_down.shape[0]
    nt = (((1,), (1,)), ((), ()))

    def body(a_ref, b_ref, g_ref, u_ref, o_ref):
        dv = lax.dot_general(a_ref[...].astype(_CD), b_ref[...].astype(_CD), nt, preferred_element_type=F32)
        gv = g_ref[...].astype(F32)
        sg = _sigmoid(gv)
        o_ref[0] = (dv * u_ref[...].astype(F32) * (sg * (1.0 + gv * (1.0 - sg)))).astype(o_ref.dtype)
        o_ref[1] = (dv * (gv * sg)).astype(o_ref.dtype)

    tile = pl.BlockSpec((tm, tn), lambda i, j: (i, j))
    return _pcall(body, name=name, grid=(S // tm, F // tn),
                  in_specs=[pl.BlockSpec((tm, D), lambda i, j: (i, 0)), pl.BlockSpec((tn, D), lambda i, j: (j, 0)),
                            tile, tile],
                  out_specs=pl.BlockSpec((2, tm, tn), lambda i, j: (0, i, j)),
                  out_shape=jax.ShapeDtypeStruct((2, S, F), _CD),
                  compiler_params=_params("parallel", "arbitrary"))(dx, w_down, gate, up)


def _split3(x):
    hi = x.astype(jnp.bfloat16)
    r1 = x - hi.astype(F32)
    mid = r1.astype(jnp.bfloat16)
    lo = (r1 - mid.astype(F32)).astype(jnp.bfloat16)
    return hi, mid, lo


def _ones_dot_left(ones, x):
    return sum(jnp.dot(ones, p, preferred_element_type=F32) for p in _split3(x))


def _ones_dot_right(x, ones):
    return sum(jnp.dot(p, ones, preferred_element_type=F32) for p in _split3(x))


def _head_sum(x):
    n = x.shape[1]
    r = lax.broadcasted_iota(jnp.int32, (n, n), 0) // HEAD_DIM
    c = lax.broadcasted_iota(jnp.int32, (n, n), 1) // HEAD_DIM
    return _ones_dot_right(x, (r == c).astype(jnp.bfloat16))


def _log_sigmoid(z):
    e = jnp.exp(-jnp.abs(z))
    t = 1.0 + e
    log1p_e = jnp.where(t == 1.0, e, jnp.log(t) * (e / jnp.where(t == 1.0, 1.0, t - 1.0)))
    return jnp.minimum(z, 0.0) - log1p_e


def _fox_cumsum(zf, bf, *, name):
    S, W = zf.shape
    nb = S // 128

    def body(z_ref, b_ref, c_ref):
        tri = (lax.broadcasted_iota(jnp.int32, (128, 128), 0) >= lax.broadcasted_iota(jnp.int32, (128, 128), 1))
        tri = tri.astype(jnp.bfloat16)

        def step(i, carry):
            rows = pl.ds(pl.multiple_of(i * 128, 128), 128)
            lf = _log_sigmoid(z_ref[rows, :] + b_ref[...])
            cb = _ones_dot_left(tri, lf) + carry
            c_ref[rows, :] = cb
            return cb[127:128, :]

        lax.fori_loop(0, nb, step, jnp.zeros((1, W), F32))

    return _pcall(body, name=name, out_shape=jax.ShapeDtypeStruct((S, W), F32),
                  compiler_params=pltpu.CompilerParams(vmem_limit_bytes=VMEM_LIMIT))(zf, bf)


def _fox_cumsum_bwd(dc, zf, bf, *, name):
    S, W = zf.shape
    nb = S // 128

    def body(dc_ref, z_ref, b_ref, dz_ref, db_ref):
        tri = (lax.broadcasted_iota(jnp.int32, (128, 128), 0) <= lax.broadcasted_iota(jnp.int32, (128, 128), 1))
        tri = tri.astype(jnp.bfloat16)

        def step(k, carry):
            tail, acc = carry
            i = nb - 1 - k
            rows = pl.ds(pl.multiple_of(i * 128, 128), 128)
            dlf = _ones_dot_left(tri, dc_ref[rows, :]) + tail
            dz = dlf * _sigmoid(-(z_ref[rows, :] + b_ref[...]))
            dz_ref[rows, :] = dz
            return dlf[0:1, :], acc + jnp.sum(dz, axis=0, keepdims=True)

        _, acc = lax.fori_loop(0, nb, step, (jnp.zeros((1, W), F32), jnp.zeros((1, W), F32)))
        db_ref[...] = acc

    return _pcall(body, name=name,
                  out_shape=[jax.ShapeDtypeStruct((S, W), F32), jax.ShapeDtypeStruct((1, W), F32)],
                  compiler_params=pltpu.CompilerParams(vmem_limit_bytes=VMEM_LIMIT))(dc, zf, bf)


def _proj_dil(h, w_qkv, *, name, tm=1024):
    S, D = h.shape
    tn = DIL_WIDTH

    def body(a_ref, b_ref, *rest):
        outs, acc = rest[:N_DIL_GROUPS], rest[N_DIL_GROUPS]
        prod = jnp.dot(a_ref[...].astype(_CD), b_ref[...].astype(_CD), preferred_element_type=F32)
        for k in range(tn // LANES):
            acc[k] = prod[:, k * LANES:(k + 1) * LANES]
        for g, (_, d) in enumerate(DIL_PAIRS):
            for half in range(DIL_OUT // LANES):
                k = g * (DIL_OUT // LANES) + half
                cols = slice(half * LANES, (half + 1) * LANES)
                for r in range(d):
                    rows = pl.ds(r, tm // d, stride=d) if d > 1 else slice(None)
                    outs[g][0, r, :, cols] = acc[k, rows, :].astype(outs[g].dtype)

    out_specs = [pl.BlockSpec((1, d, tm // d, DIL_OUT), lambda i, j: (j, 0, i, 0)) for _, d in DIL_PAIRS]
    out_shape = [jax.ShapeDtypeStruct((3, d, S // d, DIL_OUT), _CD) for _, d in DIL_PAIRS]
    outs = _pcall(body, name=name, grid=(S // tm, 3),
                  in_specs=[pl.BlockSpec((tm, D), lambda i, j: (i, 0)), pl.BlockSpec((D, tn), lambda i, j: (0, j))],
                  out_specs=out_specs, out_shape=out_shape, scratch_shapes=[pltpu.VMEM((tn // LANES, tm, LANES), F32)],
                  compiler_params=_params("parallel", "arbitrary"))(h, w_qkv)
    return [o.reshape(3, S, DIL_OUT) for o in outs]


def _dil_start(block, S, dilation):
    sub = S // dilation
    u0 = block * DIL_W
    return (u0 % sub) * dilation + u0 // sub


def _dil_slopes(group):
    h = np.arange(1, N_DIL_GROUPS * DIL_HEADS + 1, dtype=np.float32)
    s = (np.float32(2.0) ** (np.float32(-8.0) * h / np.float32(N_DIL_GROUPS * DIL_HEADS))).astype(np.float32)
    return [float(v) for v in s.reshape(N_DIL_GROUPS, DIL_HEADS)[group]]


def _dil_tiles(i, n, blocks_per_seq):
    qi = lax.broadcasted_iota(jnp.int32, (DIL_W, 2 * DIL_W), 0)
    kj = lax.broadcasted_iota(jnp.int32, (DIL_W, 2 * DIL_W), 1)
    rel = qi + DIL_W - kj
    first = ((4 * n + i) % blocks_per_seq) == 0
    valid = jnp.logical_and(jnp.logical_and(rel >= 0, rel <= DIL_W), jnp.logical_or(kj >= DIL_W, jnp.logical_not(first)))
    return valid, rel.astype(F32)


def _dil_window(cur_ref, prev_ref, i, cols):
    if i > 0:
        return cur_ref[(i - 1) * DIL_W:(i + 1) * DIL_W, cols]
    return jnp.concatenate([prev_ref[:, cols], cur_ref[:DIL_W, cols]], axis=0)


CHUNK = 4 * DIL_W


def _dil_rows(block, S, dilation):
    start = _dil_start(block, S, dilation)
    return pl.ds(start, DIL_W, stride=dilation) if dilation > 1 else pl.ds(start, DIL_W)


def SPLIT(S):
    return (DIL_OUT // LANES, S, LANES)


def _dil_fwd(qkv, group, *, name):
    S = qkv.shape[1]
    dilation = DIL_PAIRS[group][1]
    bps = (S // dilation) // DIL_W
    slopes = _dil_slopes(group)
    nt = (((1,), (1,)), ((), ()))

    def body(q_ref, k_ref, v_ref, kp_ref, vp_ref, on_ref, ln_ref, o_ref, l_ref):
        n = pl.program_id(0)
        for i in range(4):
            valid, rel = _dil_tiles(i, n, bps)
            rows = slice(i * DIL_W, (i + 1) * DIL_W)
            for h in range(DIL_HEADS):
                cols = slice(h * HEAD_DIM, (h + 1) * HEAD_DIM)
                qh = q_ref[rows, cols]
                k2, v2 = _dil_window(k_ref, kp_ref, i, cols), _dil_window(v_ref, vp_ref, i, cols)
                s = lax.dot_general(qh, k2, nt, preferred_element_type=F32) * ATTN_SCALE - (slopes[h] * dilation) * rel
                s = jnp.where(valid, s, NEG_INF)
                m = jnp.max(s, axis=-1, keepdims=True)
                p = jnp.exp(s - m)
                den = jnp.sum(p, axis=-1, keepdims=True)
                acc = jnp.dot(p.astype(_CD), v2, preferred_element_type=F32)
                o_ref[rows, cols] = acc / den
                l_ref[rows, cols] = jnp.broadcast_to(m + jnp.log(den), (DIL_W, HEAD_DIM))
        for i in range(4):
            rows = slice(i * DIL_W, (i + 1) * DIL_W)
            nat = _dil_rows(4 * n + i, S, dilation)
            for half in range(DIL_OUT // LANES):
                cols = slice(half * LANES, (half + 1) * LANES)
                on_ref[half, nat, :] = o_ref[rows, cols]
                ln_ref[half, nat, :] = l_ref[rows, cols]

    def cur(which):
        return pl.BlockSpec((None, CHUNK, DIL_OUT), lambda n: (which, n, 0))

    def prev(which):
        return pl.BlockSpec((None, DIL_W, DIL_OUT), lambda n: (which, jnp.maximum(4 * n - 1, 0), 0))

    whole = pl.BlockSpec(SPLIT(S), lambda n: (0, 0, 0))
    return _pcall(body, name=name, grid=(S // CHUNK,), in_specs=[cur(0), cur(1), cur(2), prev(1), prev(2)],
                  out_specs=[whole, whole],
                  out_shape=[jax.ShapeDtypeStruct(SPLIT(S), F32), jax.ShapeDtypeStruct(SPLIT(S), F32)],
                  scratch_shapes=[pltpu.VMEM((CHUNK, DIL_OUT), F32), pltpu.VMEM((CHUNK, DIL_OUT), F32)],
                  compiler_params=_params("arbitrary"))(qkv, qkv, qkv, qkv, qkv)


STAT_OFFSET = HEAD_DIM // 2


def _dil_bwd(qkv, stats, do, group, *, name):
    S = qkv.shape[1]
    dilation = DIL_PAIRS[group][1]
    bps = (S // dilation) // DIL_W
    slopes = _dil_slopes(group)
    nchunk = S // CHUNK
    nt = (((1,), (1,)), ((), ()))
    tn = (((0,), (0,)), ((), ()))

    def body(q_ref, k_ref, v_ref, kp_ref, vp_ref, ln_ref, don_ref, dqn_ref, dkn_ref, dvn_ref,
             dk_s, dv_s, l_ref, do_ref, dq_ref):
        step = pl.program_id(0)
        n = nchunk - 1 - step
        for i in range(4):
            rows = slice(i * DIL_W, (i + 1) * DIL_W)
            nat = _dil_rows(4 * n + i, S, dilation)
            for half in range(DIL_OUT // LANES):
                cols = slice(half * LANES, (half + 1) * LANES)
                l_ref[rows, cols] = ln_ref[half, nat, :]
                do_ref[rows, cols] = don_ref[half, nat, :]

        @pl.when(step == 0)
        def _():
            dk_s[:, CHUNK:] = jnp.zeros((DIL_OUT, DIL_W), F32)
            dv_s[:, CHUNK:] = jnp.zeros((DIL_OUT, DIL_W), F32)

        dk_s[:, :CHUNK] = jnp.zeros((DIL_OUT, CHUNK), F32)
        dv_s[:, :CHUNK] = jnp.zeros((DIL_OUT, CHUNK), F32)
        for i in range(4):
            valid, rel = _dil_tiles(i, n, bps)
            rows = slice(i * DIL_W, (i + 1) * DIL_W)
            window = slice(i * DIL_W, (i + 2) * DIL_W)
            for h in range(DIL_HEADS):
                cols = slice(h * HEAD_DIM, (h + 1) * HEAD_DIM)
                qh = q_ref[rows, cols]
                k2, v2 = _dil_window(k_ref, kp_ref, i, cols), _dil_window(v_ref, vp_ref, i, cols)
                lh = l_ref[rows, h * HEAD_DIM:h * HEAD_DIM + 1]
                shift = l_ref[rows, h * HEAD_DIM + STAT_OFFSET:h * HEAD_DIM + STAT_OFFSET + 1]
                s = lax.dot_general(qh, k2, nt, preferred_element_type=F32) * ATTN_SCALE - (slopes[h] * dilation) * rel
                p = jnp.exp(jnp.where(valid, s, NEG_INF) - lh)
                dob = do_ref[rows, cols].astype(_CD)
                ds = p * (lax.dot_general(dob, v2, nt, preferred_element_type=F32) + shift)
                dsb = (ds * ATTN_SCALE).astype(_CD)
                dq_ref[rows, cols] = jnp.dot(dsb, k2, preferred_element_type=F32)
                dk_s[cols, window] += lax.dot_general(qh, dsb, tn, preferred_element_type=F32)
                dv_s[cols, window] += lax.dot_general(dob, p.astype(_CD), tn, preferred_element_type=F32)
        for i in range(4):
            rows = slice(i * DIL_W, (i + 1) * DIL_W)
            done = slice((i + 1) * DIL_W, (i + 2) * DIL_W)
            nat = _dil_rows(4 * n + i, S, dilation)
            dkb, dvb = dk_s[:, done].T, dv_s[:, done].T
            for half in range(DIL_OUT // LANES):
                cols = slice(half * LANES, (half + 1) * LANES)
                dqn_ref[half, nat, :] = dq_ref[rows, cols]
                dkn_ref[half, nat, :] = dkb[:, cols]
                dvn_ref[half, nat, :] = dvb[:, cols]
        dk_s[:, CHUNK:] = dk_s[:, :DIL_W]
        dv_s[:, CHUNK:] = dv_s[:, :DIL_W]

    def cur(which):
        return pl.BlockSpec((None, CHUNK, DIL_OUT), lambda s: (which, nchunk - 1 - s, 0))

    def prev(which):
        return pl.BlockSpec((None, DIL_W, DIL_OUT), lambda s: (which, jnp.maximum(4 * (nchunk - 1 - s) - 1, 0), 0))

    whole = pl.BlockSpec(SPLIT(S), lambda s: (0, 0, 0))
    shp = jax.ShapeDtypeStruct(SPLIT(S), F32)
    tile = pltpu.VMEM((CHUNK, DIL_OUT), F32)
    return _pcall(body, name=name, grid=(nchunk,),
                  in_specs=[cur(0), cur(1), cur(2), prev(1), prev(2), whole, whole],
                  out_specs=[whole, whole, whole], out_shape=[shp, shp, shp],
                  scratch_shapes=[pltpu.VMEM((DIL_OUT, CHUNK + DIL_W), F32), pltpu.VMEM((DIL_OUT, CHUNK + DIL_W), F32),
                                  tile, tile, tile],
                  compiler_params=pltpu.CompilerParams(dimension_semantics=("arbitrary",),
                                                       vmem_limit_bytes=VMEM_LIMIT_RESIDENT))(
        qkv, qkv, qkv, qkv, qkv, stats, do)


def _dil_mix_fwd(os_, ls_, *, name, tm=512):
    nh, S, _ = os_[0].shape

    def body(o0, o1, o2, l0, l1, l2, out_ref):
        for half in range(nh):
            ls = [l0[half], l1[half], l2[half]]
            m = jnp.maximum(jnp.maximum(ls[0], ls[1]), ls[2])
            es = [jnp.exp(l - m) for l in ls]
            den = es[0] + es[1] + es[2]
            mixed = (es[0] * o0[half] + es[1] * o1[half] + es[2] * o2[half]) / den
            out_ref[:, half * LANES:(half + 1) * LANES] = mixed.astype(out_ref.dtype)

    halves = pl.BlockSpec((nh, tm, LANES), lambda i: (0, i, 0))
    row = pl.BlockSpec((tm, nh * LANES), lambda i: (i, 0))
    return _pcall(body, name=name, grid=(S // tm,), in_specs=[halves] * 6, out_specs=row,
                  out_shape=jax.ShapeDtypeStruct((S, nh * LANES), _CD), compiler_params=_params("parallel"))(*os_, *ls_)


def _dil_mix_bwd(doa, os_, ls_, *, name, tm=512, after=None):
    nh, S, _ = os_[0].shape

    def body(d_ref, o0, o1, o2, l0, l1, l2, do0, do1, do2, st0, st1, st2):
        first = lax.broadcasted_iota(jnp.int32, (tm, LANES), 1) % HEAD_DIM < STAT_OFFSET
        for half in range(nh):
            dv = d_ref[:, half * LANES:(half + 1) * LANES]
            ls = [l0[half], l1[half], l2[half]]
            m = jnp.maximum(jnp.maximum(ls[0], ls[1]), ls[2])
            es = [jnp.exp(l - m) for l in ls]
            den = es[0] + es[1] + es[2]
            al = [e / den for e in es]
            da = [_head_sum(dv * o[half]) for o in (o0, o1, o2)]
            mean = al[0] * da[0] + al[1] * da[1] + al[2] * da[2]
            for a, l, do_ref, st_ref in zip(al, ls, (do0, do1, do2), (st0, st1, st2)):
                do_ref[half] = a * dv
                st_ref[half] = jnp.where(first, l, -a * mean)

    halves = pl.BlockSpec((nh, tm, LANES), lambda i: (0, i, 0))
    row = pl.BlockSpec((tm, nh * LANES), lambda i: (i, 0))
    shp = jax.ShapeDtypeStruct((nh, S, LANES), F32)
    return _pcall(body, after, name=name, grid=(S // tm,), in_specs=[row] + [halves] * 6, out_specs=[halves] * 6,
                  out_shape=[shp] * 6, compiler_params=_params("parallel"))(doa, *os_, *ls_)


FOX_T = 512


PACK = 2 * HEAD_DIM
HEAD_PAIRS = N_FOX_HEADS // 2
FOX_HPS = 8
Q_BLOCK0 = 0
K_BLOCK0 = FOX_WIDTH // PACK
V_BLOCK0 = 2 * FOX_WIDTH // PACK


def _pieces(x):
    hi = x.astype(jnp.bfloat16).astype(F32)
    r = x - hi
    mid = r.astype(jnp.bfloat16).astype(F32)
    lo = (r - mid).astype(jnp.bfloat16).astype(F32)
    return [hi, mid, lo]


def _extras(first, second, rows):
    lane = lax.broadcasted_iota(jnp.int32, (rows, HEAD_DIM), 1)
    out = jnp.zeros((rows, HEAD_DIM), F32)
    for base, triple in ((0, first), (3, second)):
        if all(isinstance(v, float) for v in triple) and len(set(triple)) == 1:
            if triple[0] != 0.0:
                out = jnp.where(jnp.logical_and(lane >= base, lane < base + 3), triple[0], out)
        else:
            for idx, val in enumerate(triple):
                out = jnp.where(lane == base + idx, val, out)
    return out


def _head_column(c, h):
    lane = lax.broadcasted_iota(jnp.int32, c.shape, 1)
    return jnp.sum(jnp.where(lane == h, c, 0.0), axis=1, keepdims=True)


ONES3 = [1.0, 1.0, 1.0]
ZEROS3 = [0.0, 0.0, 0.0]


def _fox_pack_fwd(qkv, c, *, name, tm=512):
    S = qkv.shape[0]

    def body(q_ref, k_ref, v_ref, c_ref, qo_ref, ko_ref, vo_ref):
        hp = pl.program_id(1)
        cv = c_ref[...]
        v_extras = jnp.where(lax.broadcasted_iota(jnp.int32, (tm, HEAD_DIM), 1) < 3, 1.0, 0.0).astype(vo_ref.dtype)
        for hh in range(2):
            ch = _pieces(_head_column(cv, 2 * hp + hh))
            src = slice(hh * HEAD_DIM, (hh + 1) * HEAD_DIM)
            lo = slice(hh * PACK, hh * PACK + HEAD_DIM)
            hi = slice(hh * PACK + HEAD_DIM, (hh + 1) * PACK)
            qo_ref[:, lo] = (q_ref[:, src].astype(F32) * ATTN_SCALE).astype(qo_ref.dtype)
            qo_ref[:, hi] = _extras(ch, ONES3, tm).astype(qo_ref.dtype)
            ko_ref[:, lo] = k_ref[:, src]
            ko_ref[:, hi] = _extras(ONES3, [-p for p in ch], tm).astype(ko_ref.dtype)
            vo_ref[:, lo] = v_ref[:, src]
            vo_ref[:, hi] = v_extras

    def src(block0):
        return pl.BlockSpec((tm, PACK), lambda i, hp: (i, block0 + hp))

    out = pl.BlockSpec((tm, 2 * PACK), lambda i, hp: (i, hp))
    shp = jax.ShapeDtypeStruct((S, N_FOX_HEADS * PACK), _CD)
    return _pcall(body, name=name, grid=(S // tm, HEAD_PAIRS),
                  in_specs=[src(Q_BLOCK0), src(K_BLOCK0), src(V_BLOCK0), pl.BlockSpec((tm, PACK), lambda i, hp: (i, 0))],
                  out_specs=[out, out, out], out_shape=[shp, shp, shp],
                  compiler_params=_params("parallel", "parallel"))(qkv, qkv, qkv, c)


def _fox_fwd(qp, kp, vp, *, name):
    S = qp.shape[0]
    nt = S // FOX_T
    nt_dims = (((1,), (1,)), ((), ()))
    tn_dims = (((0,), (0,)), ((), ()))

    def body(i_tab, j_tab, q_ref, k_ref, v_ref, o_ref, l_ref, m_s, acc_s):
        t = pl.program_id(1)
        i, j = i_tab[t], j_tab[t]

        @pl.when(j == 0)
        def _():
            m_s[...] = jnp.full((FOX_HPS, 1, FOX_T), NEG_INF, F32)
            acc_s[...] = jnp.zeros((FOX_HPS, PACK, FOX_T), F32)

        def tile(diagonal):
            for hh in range(FOX_HPS):
                cols = slice(hh * PACK, (hh + 1) * PACK)
                st = lax.dot_general(k_ref[:, cols], q_ref[:, cols], nt_dims, preferred_element_type=F32)
                if diagonal:
                    key = lax.broadcasted_iota(jnp.int32, (FOX_T, FOX_T), 0)
                    qry = lax.broadcasted_iota(jnp.int32, (FOX_T, FOX_T), 1)
                    st = jnp.where(key <= qry, st, NEG_INF)
                m_old = m_s[hh]
                m_new = jnp.maximum(m_old, jnp.max(st, axis=0, keepdims=True))
                pt = jnp.exp(st - m_new)
                acc_s[hh] = jnp.exp(m_old - m_new) * acc_s[hh] + lax.dot_general(
                    v_ref[:, cols], pt.astype(_CD), tn_dims, preferred_element_type=F32)
                m_s[hh] = m_new

        @pl.when(j < i)
        def _():
            tile(False)

        @pl.when(j == i)
        def _():
            tile(True)
            for hh in range(FOX_HPS):
                acc = acc_s[hh]
                den = acc[HEAD_DIM:HEAD_DIM + 1, :]
                cols = slice(hh * HEAD_DIM, (hh + 1) * HEAD_DIM)
                o_ref[:, cols] = (acc[:HEAD_DIM, :] / den).T
                l_ref[:, cols] = jnp.broadcast_to(m_s[hh] + jnp.log(den), (HEAD_DIM, FOX_T)).T

    pairs = [(i, j) for i in range(nt) for j in range(i + 1)]
    i_tab = jnp.asarray([p[0] for p in pairs], jnp.int32)
    j_tab = jnp.asarray([p[1] for p in pairs], jnp.int32)
    qs = pl.BlockSpec((FOX_T, FOX_HPS * PACK), lambda hp, t, it, jt: (it[t], hp))
    ks = pl.BlockSpec((FOX_T, FOX_HPS * PACK), lambda hp, t, it, jt: (jt[t], hp))
    os_ = pl.BlockSpec((FOX_T, FOX_HPS * HEAD_DIM), lambda hp, t, it, jt: (it[t], hp))
    shp = jax.ShapeDtypeStruct((S, FOX_WIDTH), F32)
    grid_spec = pltpu.PrefetchScalarGridSpec(
        num_scalar_prefetch=2, grid=(N_FOX_HEADS // FOX_HPS, len(pairs)), in_specs=[qs, ks, ks], out_specs=[os_, os_],
        scratch_shapes=[pltpu.VMEM((FOX_HPS, 1, FOX_T), F32), pltpu.VMEM((FOX_HPS, PACK, FOX_T), F32)])
    return _pcall(body, name=name, grid_spec=grid_spec, out_shape=[shp, shp],
                  compiler_params=_params("parallel", "arbitrary"))(i_tab, j_tab, qp, kp, vp)


def _fox_pack_bwd(qkv, c, o, lse, do, *, name, tm=512, after=None):
    S = qkv.shape[0]

    def body(q_ref, c_ref, o_ref, l_ref, do_ref, qo_ref, do_out_ref):
        hp = pl.program_id(1)
        cv = c_ref[...]
        for hh in range(2):
            src = slice(hh * HEAD_DIM, (hh + 1) * HEAD_DIM)
            lo = slice(hh * PACK, hh * PACK + HEAD_DIM)
            hi = slice(hh * PACK + HEAD_DIM, (hh + 1) * PACK)
            shift = _head_column(cv, 2 * hp + hh) - l_ref[:, hh * HEAD_DIM:hh * HEAD_DIM + 1]
            dov = do_ref[:, src]
            dsum = jnp.sum(dov * o_ref[:, src], axis=-1, keepdims=True)
            qo_ref[:, lo] = (q_ref[:, src].astype(F32) * ATTN_SCALE).astype(qo_ref.dtype)
            qo_ref[:, hi] = _extras(_pieces(shift), ONES3, tm).astype(qo_ref.dtype)
            do_out_ref[:, lo] = dov.astype(do_out_ref.dtype)
            do_out_ref[:, hi] = _extras(_pieces(-dsum), ZEROS3, tm).astype(do_out_ref.dtype)

    pair = pl.BlockSpec((tm, PACK), lambda i, hp: (i, hp))
    out = pl.BlockSpec((tm, 2 * PACK), lambda i, hp: (i, hp))
    shp = jax.ShapeDtypeStruct((S, N_FOX_HEADS * PACK), _CD)
    return _pcall(body, after, name=name, grid=(S // tm, HEAD_PAIRS),
                  in_specs=[pl.BlockSpec((tm, PACK), lambda i, hp: (i, Q_BLOCK0 + hp)),
                            pl.BlockSpec((tm, PACK), lambda i, hp: (i, 0)), pair, pair, pair],
                  out_specs=[out, out], out_shape=[shp, shp],
                  compiler_params=_params("parallel", "parallel"))(qkv, c, o, lse, do)


def _fox_bwd(qp, kp, vp, dop, *, name):
    S = qp.shape[0]
    nt = S // FOX_T
    nt_dims = (((1,), (1,)), ((), ()))
    tn_dims = (((0,), (0,)), ((), ()))

    def body(i_tab, j_tab, q_ref, k_ref, v_ref, do_ref, dq_ref, dk_ref, dv_ref, dc_ref, dr_ref,
             dq_s, dk_s, dv_s, dc_s, dr_s):
        t = pl.program_id(1)
        i, j = i_tab[t], j_tab[t]

        @pl.when(t == 0)
        def _():
            dq_s[...] = jnp.zeros((S, FOX_HPS * PACK), F32)
            dr_s[...] = jnp.zeros((FOX_HPS, 1, S), F32)

        @pl.when(i == j)
        def _():
            dk_s[...] = jnp.zeros((FOX_T, FOX_HPS * PACK), F32)
            dv_s[...] = jnp.zeros((FOX_T, FOX_HPS * PACK), F32)
            dc_s[...] = jnp.zeros((FOX_HPS, FOX_T, 1), F32)

        def tile(diagonal):
            rows = pl.ds(pl.multiple_of(i * FOX_T, FOX_T), FOX_T)
            for hh in range(FOX_HPS):
                cols = slice(hh * PACK, (hh + 1) * PACK)
                qv, kv, vv, dov = q_ref[:, cols], k_ref[:, cols], v_ref[:, cols], do_ref[:, cols]
                pt = jnp.exp(lax.dot_general(kv, qv, nt_dims, preferred_element_type=F32))
                if diagonal:
                    key = lax.broadcasted_iota(jnp.int32, (FOX_T, FOX_T), 0)
                    qry = lax.broadcasted_iota(jnp.int32, (FOX_T, FOX_T), 1)
                    pt = jnp.where(key <= qry, pt, 0.0)
                dst = pt * lax.dot_general(vv, dov, nt_dims, preferred_element_type=F32)
                dsb = dst.astype(_CD)
                dc_s[hh] += jnp.sum(dst, axis=1, keepdims=True)
                dr_s[hh, :, rows] += jnp.sum(dst, axis=0, keepdims=True)
                dv_s[:, cols] += jnp.dot(pt.astype(_CD), dov, preferred_element_type=F32)
                dk_s[:, cols] += jnp.dot(dsb, qv, preferred_element_type=F32)
                dq_s[rows, cols] += lax.dot_general(dsb, kv, tn_dims, preferred_element_type=F32)

        @pl.when(i > j)
        def _():
            tile(False)

        @pl.when(i == j)
        def _():
            tile(True)

        @pl.when(i == nt - 1)
        def _():
            for hh in range(FOX_HPS):
                src = slice(hh * PACK, hh * PACK + HEAD_DIM)
                dst_cols = slice(hh * HEAD_DIM, (hh + 1) * HEAD_DIM)
                dk_ref[:, dst_cols] = dk_s[:, src].astype(dk_ref.dtype)
                dv_ref[:, dst_cols] = dv_s[:, src].astype(dv_ref.dtype)
                dc_ref[:, dst_cols] = jnp.broadcast_to(dc_s[hh], (FOX_T, HEAD_DIM))

        @pl.when(t == len(pairs) - 1)
        def _():
            for hh in range(FOX_HPS):
                dq_ref[:, hh * HEAD_DIM:(hh + 1) * HEAD_DIM] = (
                    dq_s[:, hh * PACK:hh * PACK + HEAD_DIM] * ATTN_SCALE).astype(dq_ref.dtype)
            dr_ref[...] = dr_s[...]

    pairs = [(i, j) for j in range(nt) for i in range(j, nt)]
    i_tab = jnp.asarray([p[0] for p in pairs], jnp.int32)
    j_tab = jnp.asarray([p[1] for p in pairs], jnp.int32)
    wide, narrow = FOX_HPS * PACK, FOX_HPS * HEAD_DIM
    qs = pl.BlockSpec((FOX_T, wide), lambda hp, t, it, jt: (it[t], hp))
    ks = pl.BlockSpec((FOX_T, wide), lambda hp, t, it, jt: (jt[t], hp))
    whole = pl.BlockSpec((S, narrow), lambda hp, t, it, jt: (0, hp))
    cs = pl.BlockSpec((FOX_T, narrow), lambda hp, t, it, jt: (jt[t], hp))
    rs = pl.BlockSpec((FOX_HPS, 1, S), lambda hp, t, it, jt: (hp, 0, 0))
    shp = jax.ShapeDtypeStruct((S, FOX_WIDTH), _CD)
    grid_spec = pltpu.PrefetchScalarGridSpec(
        num_scalar_prefetch=2, grid=(N_FOX_HEADS // FOX_HPS, len(pairs)), in_specs=[qs, ks, ks, qs],
        out_specs=[whole, cs, cs, cs, rs],
        scratch_shapes=[pltpu.VMEM((S, wide), F32), pltpu.VMEM((FOX_T, wide), F32),
                        pltpu.VMEM((FOX_T, wide), F32), pltpu.VMEM((FOX_HPS, FOX_T, 1), F32),
                        pltpu.VMEM((FOX_HPS, 1, S), F32)])
    return _pcall(body, name=name, grid_spec=grid_spec,
                  out_shape=[shp, shp, shp, jax.ShapeDtypeStruct((S, FOX_WIDTH), F32),
                             jax.ShapeDtypeStruct((N_FOX_HEADS, 1, S), F32)],
                  compiler_params=_params("parallel", "arbitrary"))(i_tab, j_tab, qp, kp, vp, dop)


def _layer_step(x, tgt, w, p, late_weights=None, grad_sink=None, after=None, first_weights=None):
    S = x.shape[0]
    after_norm, after_proj = after if after is not None else (None, None)
    h = _rms_fwd(x, p["norm_mix_g"], name="rms_mix", after=after_norm)
    if first_weights is not None:
        w = {**w, **first_weights(h)}
    qkv = _mm(h, w["qkv"][:, 3 * DIL_WIDTH:], name="proj_fox", out_dtype=_CD, tn=768, tm=2048, after=after_proj)
    dil_qkv = _proj_dil(h, w["qkv"], name="proj_dil")
    zf = _mm(h, w["f"], name="proj_f")
    gl = _mm(h, w["g"], name="proj_gate", tn=1024, out_dtype=_CD)

    dil_o, dil_l = [], []
    for g in range(N_DIL_GROUPS):
        og, lg = _dil_fwd(dil_qkv[g], g, name=f"dil_fwd{g}")
        dil_o.append(og), dil_l.append(lg)
    o_a = _dil_mix_fwd(dil_o, dil_l, name="dil_mix")

    c = _fox_cumsum(zf, p["b_fgt"], name="fox_cumsum")
    fqp, fkp, fvp = _fox_pack_fwd(qkv, c, name="fox_pack")
    o_b, flse = _fox_fwd(fqp, fkp, fvp, name="fox_fwd")

    if late_weights is not None:
        w = {**w, **late_weights(o_b)}
    y_a = _mm(o_a, w["dil_out"], name="y_a", tn=1024, out_dtype=_CD)
    y_b = _mm(o_b, w["fox_out"], name="y_b", tn=1024, out_dtype=_CD)
    merged = _gate_fwd(gl, p["b_gate"], y_a, y_b, name="gate_fwd")
    x1 = _mm(merged, w["out"], name="mix_out", add=x)

    h2 = _rms_fwd(x1, p["norm_ffn_g"], name="rms_ffn")
    gate, up, act = _ffn_in_act(h2, w["ffn_in"], name="ffn_in")
    x2 = _mm(act, w["ffn_down"], name="ffn_down", add=x1, tk=2816)

    loss, dx2, dg_final = _loss_head(x2, p["norm_final_g"], tgt, name="loss_head")

    gw_ffn_down = _mm(act, dx2, name="gw_ffn_down", ta=True, out_dtype=_CD, tm=1408)
    dgu = _d_swiglu(dx2, w["ffn_down"], gate, up, name="d_swiglu")
    dh2 = _mm(dgu, w["ffn_in"], name="d_h2", tb=True, tk=1408, b_blocks=True, tm=2048, a_halves=True)
    gw_ffn_in = _mm(h2, dgu, name="gw_ffn_in", ta=True, out_dtype=_CD, tn=1408, out_blocks=1408, b_halves=True)
    sink = grad_sink if grad_sink is not None else (lambda group, grads: None)
    tok = sink("ffn", dict(ffn_in=gw_ffn_in, ffn_down=gw_ffn_down))
    dx1, dg_ffn = _rms_bwd(x1, p["norm_ffn_g"], dh2, dx2, name="rms_ffn_bwd", after=tok)

    dmerged = _mm(dx1, w["out"], name="d_merged", tb=True, out_dtype=_CD)
    gw_out = _mm(merged, dx1, name="gw_out", ta=True, out_dtype=_CD)
    dy_a, dy_b, dgl, db_gate = _gate_bwd(dmerged, gl, p["b_gate"], y_a, y_b, name="gate_bwd")
    do_a = _mm(dy_a, w["dil_out"], name="d_o_a", tb=True)
    gw_dil_out = _mm(o_a, dy_a, name="gw_dil_out", ta=True, out_dtype=_CD, tn=1024)
    do_b = _mm(dy_b, w["fox_out"], name="d_o_b", tb=True)
    gw_fox_out = _mm(o_b, dy_b, name="gw_fox_out", ta=True, out_dtype=_CD, tn=1024)
    tok = sink("mix", dict(dil_out=gw_dil_out, fox_out=gw_fox_out, out=gw_out))

    bqp, bdop = _fox_pack_bwd(qkv, c, o_b, flse, do_b, name="fox_pack_bwd", after=tok)
    dqp, dkp, dvp, dck, dcq = _fox_bwd(bqp, fkp, fvp, bdop, name="fox_bwd")
    dc = dcq[:, 0, :].T - dck.reshape(S, N_FOX_HEADS, HEAD_DIM)[:, :, 0]
    dc = jnp.pad(dc, ((0, 0), (0, F_PAD - N_FOX_HEADS)))
    dzf, db_fgt = _fox_cumsum_bwd(dc, zf, p["b_fgt"], name="fox_cumsum_bwd")

    douts = _dil_mix_bwd(do_a, dil_o, dil_l, name="dil_mix_bwd", after=tok)
    dqs, dks, dvs = [], [], []
    for g in range(N_DIL_GROUPS):
        dq, dk, dv = _dil_bwd(dil_qkv[g], douts[3 + g], douts[g], g, name=f"dil_bwd{g}")
        for parts, t in ((dqs, dq), (dks, dk), (dvs, dv)):
            parts.extend([t[0].astype(_CD), t[1].astype(_CD)])
    dqkv = jnp.concatenate(dqs + dks + dvs + [dqp, dkp, dvp], axis=1)

    gw_qkv = _mm(h, dqkv, name="gw_qkv", ta=True, out_dtype=_CD, tn=768)
    gw_g = _mm(h, dgl, name="gw_gate", ta=True, out_dtype=_CD)
    gw_f = _mm(h, dzf, name="gw_f", ta=True, out_dtype=_CD)
    tok = sink("in", dict(qkv=gw_qkv, f=gw_f, g=gw_g))
    dh = _mm(dqkv, w["qkv"], name="d_h_qkv", tb=True, tk=1920, tm=2048, after=tok)
    dh = _mm(dgl, w["g"], name="d_h_gate", tb=True, add=dh)
    dh = _mm(dzf, w["f"], name="d_h_f", tb=True, add=dh)
    dx, dg_mix = _rms_bwd(x, p["norm_mix_g"], dh, dx1, name="rms_mix_bwd")

    gw = dict(qkv=gw_qkv, f=gw_f, g=gw_g, dil_out=gw_dil_out, fox_out=gw_fox_out, out=gw_out, ffn_in=gw_ffn_in,
              ffn_down=gw_ffn_down)
    small = dict(norm_mix_g=dg_mix, b_fgt=db_fgt, b_gate=db_gate, norm_ffn_g=dg_ffn, norm_final_g=dg_final)
    return loss, dx, gw, small


def _position():
    return lax.axis_index("x"), lax.axis_index("y"), lax.axis_index("c")


def _other_chips(x, y):
    return [(1 - x, y), (x, 1 - y), (1 - x, 1 - y)]


ROW_TILE = 16


def _row_chunks(rows, want=4):
    n = want
    while n > 1 and rows % (n * ROW_TILE):
        n //= 2
    return n


SEM_SPEC = pl.BlockSpec(memory_space=pltpu.SEMAPHORE)
ANY_SPEC = pl.BlockSpec(memory_space=pl.ANY)
DATAFLOW = pltpu.SideEffectType.DATAFLOW_SIDE_EFFECTING


def _in_hbm(a):
    return pltpu.with_memory_space_constraint(a, pltpu.HBM)


def _split_copy_start(srcs, land_shapes, copies, after, *, name):
    n, m = len(srcs), len(land_shapes)

    def body(*refs):
        src_refs, land_refs = refs[:n], refs[n:n + m]
        send_sems, recv_sems = refs[n + m + 1], refs[n + m + 2]
        token = refs[-1]
        x, y, c = _position()
        for k, (src, dst, peer) in enumerate(copies(x, y, c, src_refs, land_refs)):
            pltpu.make_async_remote_copy(src_ref=src, dst_ref=dst, send_sem=send_sems.at[k], recv_sem=recv_sems.at[k],
                                         device_id=peer, device_id_type=MESH).start()
        token[...] = jnp.zeros_like(token)

    lands = [lax.empty(s.shape, s.dtype) for s in land_shapes]
    count = len(copies(0, 0, 0, srcs, lands))
    out = _pcall(
        body, name=name,
        out_shape=(pltpu.SemaphoreType.DMA((count,)), pltpu.SemaphoreType.DMA((count,)),
                   *[pltpu.HBM(s.shape, s.dtype) for s in srcs], *[pltpu.HBM(s.shape, s.dtype) for s in land_shapes],
                   jax.ShapeDtypeStruct((8, 128), F32)),
        in_specs=[HBM_SPEC] * (n + m) + [ANY_SPEC],
        out_specs=(SEM_SPEC, SEM_SPEC, *[HBM_SPEC] * (n + m), pl.BlockSpec(memory_space=pltpu.VMEM)),
        input_output_aliases={k: 2 + k for k in range(n + m)},
        compiler_params=pltpu.CompilerParams(has_side_effects=DATAFLOW),
    )(*[_in_hbm(s) for s in srcs], *[_in_hbm(l) for l in lands], after)
    return out[0], out[1], list(out[2:2 + n]), list(out[2 + n:2 + n + m]), out[-1]


def _split_copy_wait(send_sems, recv_sems, srcs, lands, copies, after, *, name):
    n, m = len(srcs), len(lands)

    def body(*refs):
        src_refs, land_refs = refs[:n], refs[n:n + m]
        send, recv = refs[n + m], refs[n + m + 1]
        x, y, c = _position()
        for k, (src, dst, peer) in enumerate(copies(x, y, c, src_refs, land_refs)):
            cp = pltpu.make_async_remote_copy(src_ref=src, dst_ref=dst, send_sem=send.at[k], recv_sem=recv.at[k],
                                              device_id=peer, device_id_type=MESH)
            cp.wait_send()
            cp.wait_recv()

    afters = list(after) if isinstance(after, (list, tuple)) else [after]
    out = _pcall(
        body, name=name,
        out_shape=tuple(pltpu.HBM(s.shape, s.dtype) for s in list(srcs) + list(lands)),
        in_specs=[HBM_SPEC] * (n + m) + [SEM_SPEC, SEM_SPEC] + [ANY_SPEC] * len(afters),
        out_specs=tuple([HBM_SPEC] * (n + m)),
        input_output_aliases={k: k for k in range(n + m)},
        compiler_params=pltpu.CompilerParams(has_side_effects=DATAFLOW),
    )(*srcs, *lands, send_sems, recv_sems, *afters)
    return list(out[:n]), list(out[n:])


def _gather_copies(x, y, c, shard_refs, land_refs):
    out = []
    for s, l in zip(shard_refs, land_refs):
        half = s.shape[0] // 2
        nq = _row_chunks(half)
        for cx, cy in _other_chips(x, y):
            for q in range(nq):
                rows = pl.ds(c * half + q * (half // nq), half // nq)
                out.append((s.at[rows, :], l.at[2 * x + y, rows, :], (cx, cy, c)))
    return out


def _gather_whole_copies(x, y, c, shard_refs, land_refs):
    out = []
    for s, l in zip(shard_refs, land_refs):
        nq = _row_chunks(s.shape[0])
        for cx, cy in _other_chips(x, y):
            for q in range(nq):
                rows = pl.ds(q * (s.shape[0] // nq), s.shape[0] // nq)
                out.append((s.at[rows, :], l.at[2 * x + y, rows, :], (cx, cy, c)))
    return out


def _scatter_copies(x, y, c, part_refs, land_refs):
    out = []
    for p, l in zip(part_refs, land_refs):
        nq = _row_chunks(p.shape[1])
        for r, (cx, cy) in enumerate(_other_chips(x, y)):
            for q in range(nq):
                rows = pl.ds(q * (p.shape[1] // nq), p.shape[1] // nq)
                out.append((p.at[2 * cx + cy, rows, :], l.at[r, rows, :], (cx, cy, c)))
    return out


def _scatter_all_copies(x, y, c, block_refs, land_refs):
    out = []
    for g, l in zip(block_refs, land_refs):
        half = g.shape[1] // 2
        nq = _row_chunks(half)
        size = half // nq
        for q in range(nq):
            rows = pl.ds((1 - c) * half + q * size, size)
            out.append((g.at[2 * x + y, rows, :], l.at[0, pl.ds(q * size, size), :], (x, y, 1 - c)))
        for r, (cx, cy) in enumerate(_other_chips(x, y)):
            for j in range(2):
                h = c if j == 0 else 1 - c
                for q in range(nq):
                    rows = pl.ds(h * half + q * size, size)
                    out.append((g.at[2 * cx + cy, rows, :], l.at[1 + 2 * r + j, pl.ds(q * size, size), :], (cx, cy, h)))
    return out


def _forward_halves(lands, *, name):
    n = len(lands)

    def body(*refs):
        ins = refs[:n]
        send_sems, recv_sems = refs[2 * n:]
        x, y, c = _position()
        copies = []
        for w in range(n):
            half = ins[w].shape[1] // 2
            for r, (cx, cy) in enumerate(_other_chips(x, y)):
                blk = ins[w].at[2 * cx + cy, pl.ds(c * half, half), :]
                cp = pltpu.make_async_remote_copy(src_ref=blk, dst_ref=blk, send_sem=send_sems.at[w, r],
                                                  recv_sem=recv_sems.at[w, r], device_id=(x, y, 1 - c),
                                                  device_id_type=MESH)
                cp.start()
                copies.append(cp)
        for w in range(n):
            half = ins[w].shape[1] // 2
            for r, (cx, cy) in enumerate(_other_chips(x, y)):
                blk = ins[w].at[2 * cx + cy, pl.ds((1 - c) * half, half), :]
                pltpu.make_async_remote_copy(src_ref=blk, dst_ref=blk, send_sem=send_sems.at[w, r],
                                             recv_sem=recv_sems.at[w, r], device_id=(x, y, 1 - c),
                                             device_id_type=MESH).wait_recv()
        for cp in copies:
            cp.wait_send()

    return _pcall(
        body, name=name, in_specs=[HBM_SPEC] * n, out_specs=[HBM_SPEC] * n,
        out_shape=[jax.ShapeDtypeStruct(l.shape, l.dtype) for l in lands],
        input_output_aliases={k: k for k in range(n)},
        scratch_shapes=[pltpu.SemaphoreType.DMA((n, 3)), pltpu.SemaphoreType.DMA((n, 3))],
    )(*lands)


def _swap_halves(grads, name="swap_halves"):
    n = len(grads)

    def body(*refs):
        ins, outs = refs[:n], refs[n:2 * n]
        send_sems, recv_sems = refs[2 * n:]
        x, y, c = _position()
        copies = []
        for w in range(n):
            half = ins[w].shape[1] // 2
            cp = pltpu.make_async_remote_copy(
                src_ref=ins[w].at[:, pl.ds((1 - c) * half, half), :], dst_ref=outs[w], send_sem=send_sems.at[w],
                recv_sem=recv_sems.at[w], device_id=(x, y, 1 - c), device_id_type=MESH)
            cp.start()
            copies.append(cp)
        for cp in copies:
            cp.wait()

    return _pcall(
        body, name=name, in_specs=[HBM_SPEC] * n, out_specs=[HBM_SPEC] * n,
        out_shape=[jax.ShapeDtypeStruct((4, g.shape[1] // 2, g.shape[2]), g.dtype) for g in grads],
        scratch_shapes=[pltpu.SemaphoreType.DMA((n,)), pltpu.SemaphoreType.DMA((n,))],
    )(*grads)


def _share_halves(halves):
    n = len(halves)

    def body(*refs):
        ins, outs = refs[:n], refs[n:2 * n]
        send_sems, recv_sems = refs[2 * n:]
        x, y, c = _position()
        copies = []
        for w in range(n):
            cp = pltpu.make_async_remote_copy(src_ref=ins[w], dst_ref=outs[w], send_sem=send_sems.at[w],
                                              recv_sem=recv_sems.at[w], device_id=(x, y, 1 - c), device_id_type=MESH)
            cp.start()
            copies.append(cp)
        for cp in copies:
            cp.wait()

    return _pcall(
        body, name="share_halves", in_specs=[HBM_SPEC] * n, out_specs=[HBM_SPEC] * n,
        out_shape=[jax.ShapeDtypeStruct(h.shape, h.dtype) for h in halves],
        scratch_shapes=[pltpu.SemaphoreType.DMA((n,)), pltpu.SemaphoreType.DMA((n,))],
    )(*halves)


def _sum_small(part):
    rows, width = part.shape

    def body(x_ref, out_ref, all_ref, send_sems, recv_sems):
        x, y, c = _position()
        me, sibling = (x, y, c), (x, y, 1 - c)
        chips = _other_chips(x, y)

        def block(px, py, pc):
            return all_ref.at[pl.ds((4 * px + 2 * py + pc) * rows, rows), :]

        def copy(k, blk, to, src=None):
            return pltpu.make_async_remote_copy(
                src_ref=block(*blk) if src is None else src, dst_ref=block(*blk), send_sem=send_sems.at[k],
                recv_sem=recv_sems.at[k], device_id=to, device_id_type=MESH)

        all_ref[pl.ds((4 * x + 2 * y + c) * rows, rows), :] = x_ref[...]
        first = [copy(0, me, sibling, src=x_ref)]
        first += [copy(1 + j, me, (*chip, c), src=x_ref) for j, chip in enumerate(chips)]
        for cp in first:
            cp.start()
        passed = [copy(4 + j, (*chip, c), sibling) for j, chip in enumerate(chips)]
        for j, chip in enumerate(chips):
            copy(1 + j, (*chip, c), me).wait_recv()
            passed[j].start()
        copy(0, sibling, me).wait_recv()
        for j, chip in enumerate(chips):
            copy(4 + j, (*chip, 1 - c), me).wait_recv()
        for cp in first + passed:
            cp.wait_send()
        total = all_ref[0:rows, :]
        for d in range(1, 8):
            total = total + all_ref[d * rows:(d + 1) * rows, :]
        out_ref[...] = total

    vm = pl.BlockSpec(memory_space=pltpu.VMEM)
    return _pcall(
        body, name="sum_small", in_specs=[vm], out_specs=vm, out_shape=jax.ShapeDtypeStruct((rows, width), F32),
        scratch_shapes=[pltpu.VMEM((8 * rows, width), F32), pltpu.SemaphoreType.DMA((7,)), pltpu.SemaphoreType.DMA((7,))],
    )(part)


def _row_tile(R, C, itemsize=4, budget=1 << 20):
    for t in (512, 256, 128, 64, 32, 16, 8):
        if R % t == 0 and t * C * itemsize <= budget:
            return t
    return R


def _add_halves(g, recv, c, *, name):
    _, R, C = g.shape
    half = R // 2
    t = _row_tile(half, C)
    nb = half // t

    def body(c_ref, g_ref, r_ref, o_ref):
        o_ref[...] = (g_ref[...].astype(F32) + r_ref[...].astype(F32)).astype(o_ref.dtype)

    grid_spec = pltpu.PrefetchScalarGridSpec(
        num_scalar_prefetch=1, grid=(4, nb),
        in_specs=[pl.BlockSpec((1, t, C), lambda k, i, cr: (k, cr[0] * nb + i, 0)),
                  pl.BlockSpec((1, t, C), lambda k, i, cr: (k, i, 0))],
        out_specs=pl.BlockSpec((1, t, C), lambda k, i, cr: (k, i, 0)))
    return _pcall(body, name=name, grid_spec=grid_spec, out_shape=jax.ShapeDtypeStruct((4, half, C), g.dtype),
                  compiler_params=_params("parallel", "parallel"))(c, g, recv)


def _add_all(g, recv, where, *, name):
    _, R, C = g.shape
    half = R // 2
    t = _row_tile(half, C)
    nb = half // t

    def body(w_ref, g_ref, r_ref, o_ref):
        total = g_ref[0].astype(F32)
        for k in range(7):
            total = total + r_ref[k].astype(F32)
        o_ref[...] = total

    grid_spec = pltpu.PrefetchScalarGridSpec(
        num_scalar_prefetch=1, grid=(nb,),
        in_specs=[pl.BlockSpec((1, t, C), lambda i, wr: (wr[0], wr[1] * nb + i, 0)),
                  pl.BlockSpec((7, t, C), lambda i, wr: (0, i, 0))],
        out_specs=pl.BlockSpec((t, C), lambda i, wr: (i, 0)))
    return _pcall(body, name=name, grid_spec=grid_spec, out_shape=jax.ShapeDtypeStruct((half, C), F32),
                  compiler_params=_params("parallel"))(where, g, recv)


def _add_owners(mine, recv, *, name):
    half, C = mine.shape
    t = _row_tile(half, C)

    def body(m_ref, r_ref, o_ref):
        o_ref[...] = ((m_ref[...].astype(F32) + r_ref[0].astype(F32)) + r_ref[1].astype(F32)) + r_ref[2].astype(F32)

    return _pcall(body, name=name, grid=(half // t,),
                  in_specs=[pl.BlockSpec((t, C), lambda i: (i, 0)), pl.BlockSpec((3, t, C), lambda i: (0, i, 0))],
                  out_specs=pl.BlockSpec((t, C), lambda i: (i, 0)), out_shape=jax.ShapeDtypeStruct((half, C), F32),
                  compiler_params=_params("parallel"))(mine, recv)


def _adamw(w, g, m, v, *, name):
    R, C = w.shape
    t = _row_tile(R, C)
    c1 = 1.0 - ADAM_B1 ** ADAM_STEP
    c2 = 1.0 - ADAM_B2 ** ADAM_STEP

    def body(w_ref, g_ref, m_ref, v_ref, d_ref, nm_ref, nv_ref):
        gv = g_ref[...]
        mn = ADAM_B1 * m_ref[...] + (1.0 - ADAM_B1) * gv
        vn = ADAM_B2 * v_ref[...] + (1.0 - ADAM_B2) * (gv * gv)
        d_ref[...] = -ADAM_LR * ((mn / c1) / (jnp.sqrt(vn / c2) + ADAM_EPS) + ADAM_WD * w_ref[...])
        nm_ref[...] = mn
        nv_ref[...] = vn

    blk = pl.BlockSpec((t, C), lambda i: (i, 0))
    shp = jax.ShapeDtypeStruct((R, C), F32)
    return _pcall(body, name=name, grid=(R // t,), in_specs=[blk] * 4, out_specs=[blk] * 3, out_shape=[shp] * 3,
                  compiler_params=_params("parallel"))(w, g, m, v)


BIG = ("w_in", "w_dil_out", "w_fox_out", "w_out", "w_ffn_in", "w_ffn_down")
SMALL = ("norm_mix_g", "b_fgt", "b_gate", "norm_ffn_g", "norm_final_g")
ORDER = ("norm_mix_g", "w_in", "b_fgt", "b_gate", "w_dil_out", "w_fox_out", "w_out", "norm_ffn_g", "w_ffn_in",
         "w_ffn_down", "norm_final_g")
SMALL_ROWS = {"norm_mix_g": (0, 1), "b_gate": (1, 3), "norm_ffn_g": (3, 4), "norm_final_g": (4, 5), "b_fgt": (5, 6)}


def _columns_to_blocks(full, ncol):
    K = full.shape[0]
    return full.reshape(K, 4, ncol).transpose(1, 0, 2)


def _blocks_to_columns(blocks):
    n, K, ncol = blocks.shape
    return blocks.transpose(1, 0, 2).reshape(K, n * ncol)


def kernel(x, norm_mix_g, w_in, b_fgt, b_gate, w_dil_out, w_fox_out, w_out, norm_ffn_g, w_ffn_in, w_ffn_down, norm_final_g, loss_target, m_norm_mix_g, m_w_in, m_b_fgt, m_b_gate, m_w_dil_out, m_w_fox_out, m_w_out, m_norm_ffn_g, m_w_ffn_in, m_w_ffn_down, m_norm_final_g, v_norm_mix_g, v_w_in, v_b_fgt, v_b_gate, v_w_dil_out, v_w_fox_out, v_w_out, v_norm_ffn_g, v_w_ffn_in, v_w_ffn_down, v_norm_final_g):
    weights = dict(norm_mix_g=norm_mix_g, w_in=w_in, b_fgt=b_fgt, b_gate=b_gate, w_dil_out=w_dil_out,
                   w_fox_out=w_fox_out, w_out=w_out, norm_ffn_g=norm_ffn_g, w_ffn_in=w_ffn_in, w_ffn_down=w_ffn_down,
                   norm_final_g=norm_final_g)
    m_in = dict(norm_mix_g=m_norm_mix_g, w_in=m_w_in, b_fgt=m_b_fgt, b_gate=m_b_gate, w_dil_out=m_w_dil_out,
                w_fox_out=m_w_fox_out, w_out=m_w_out, norm_ffn_g=m_norm_ffn_g, w_ffn_in=m_w_ffn_in,
                w_ffn_down=m_w_ffn_down, norm_final_g=m_norm_final_g)
    v_in = dict(norm_mix_g=v_norm_mix_g, w_in=v_w_in, b_fgt=v_b_fgt, b_gate=v_b_gate, w_dil_out=v_w_dil_out,
                w_fox_out=v_w_fox_out, w_out=v_w_out, norm_ffn_g=v_norm_ffn_g, w_ffn_in=v_w_ffn_in,
                w_ffn_down=v_w_ffn_down, norm_final_g=v_norm_final_g)
    c = lax.axis_index("c")
    chip = 2 * lax.axis_index("x") + lax.axis_index("y")

    shards = {n: weights[n][0].astype(_CD) for n in BIG}
    in_shape = jax.ShapeDtypeStruct((4,) + shards["w_in"].shape, _CD)
    send_i, recv_i, in_src, in_land, token_in = _split_copy_start(
        [shards["w_in"]], [in_shape], _gather_copies, norm_mix_g, name="gather_in_start")
    late = BIG[1:]
    send_g, recv_g, late_src, late_land, token = _split_copy_start(
        [shards[n] for n in late], [jax.ShapeDtypeStruct((4,) + shards[n].shape, _CD) for n in late],
        _gather_whole_copies, token_in, name="gather_late_start")
    adam_in = [t[0] + token_in[0, 0] for t in (w_in, m_w_in, v_w_in)]
    p = dict(norm_mix_g=norm_mix_g, b_fgt=jnp.pad(b_fgt, ((0, 0), (0, F_PAD - N_FOX_HEADS))), b_gate=b_gate,
             norm_ffn_g=norm_ffn_g, norm_final_g=norm_final_g.reshape(1, D_MODEL))

    def first_weights(after):
        own, lands = _split_copy_wait(send_i, recv_i, in_src, in_land, _gather_copies, [after] + adam_in,
                                      name="gather_in_wait")
        (g_in,) = _forward_halves(lands, name="gather_in_forward")
        full_in = _blocks_to_columns(lax.dynamic_update_index_in_dim(g_in, own[0], chip, 0))
        o3 = QKV_COLS
        o4 = o3 + N_FOX_HEADS
        return dict(qkv=full_in[:, :o3], f=jnp.pad(full_in[:, o3:o4], ((0, 0), (0, F_PAD - N_FOX_HEADS))),
                    g=full_in[:, o4:])

    def late_weights(after):
        own, lands = _split_copy_wait(send_g, recv_g, late_src, late_land, _gather_whole_copies, after,
                                      name="gather_late_wait")
        g_dil, g_fox, g_out, g_ffn_in, g_ffn_down = [
            lax.dynamic_update_index_in_dim(l, s, chip, 0) for l, s in zip(lands, own)]
        return dict(dil_out=_blocks_to_columns(g_dil), fox_out=_blocks_to_columns(g_fox),
                    out=g_out.reshape(D_MODEL, D_MODEL), ffn_in=g_ffn_in,
                    ffn_down=g_ffn_down.reshape(D_FF, D_MODEL))

    def to_blocks(n, full):
        shape = weights[n].shape
        if full.ndim == 3:
            return full
        if n in ("w_out", "w_ffn_down"):
            return full.reshape(4, shape[1], shape[2])
        return _columns_to_blocks(full, shape[2])

    in_flight = {}

    def grad_sink(group, gw):
        if group == "in":
            named = {"w_in": jnp.concatenate([gw["qkv"], gw["f"][:, :N_FOX_HEADS], gw["g"]], axis=1)}
        else:
            named = {"w_" + k: v for k, v in gw.items()}
        srcs = [to_blocks(n, named[n]) for n in named]
        lands = [jax.ShapeDtypeStruct((7, s.shape[1] // 2, s.shape[2]), s.dtype) for s in srcs]
        started = _split_copy_start(srcs, lands, _scatter_all_copies, next(iter(gw.values())),
                                    name=f"scatter_{group}_start")
        in_flight[group] = (list(named), started)
        return started[-1]

    loss_part, grad_x, gw, small = _layer_step(x[0], loss_target[0], {}, p, late_weights, grad_sink,
                                               (token_in, token), first_weights)

    halves = {}
    where = jnp.stack([chip, c]).astype(jnp.int32)
    for group, (names, (send_s, recv_s, srcs, lands, _)) in in_flight.items():
        srcs, recv = _split_copy_wait(send_s, recv_s, srcs, lands, _scatter_all_copies, grad_x,
                                      name=f"scatter_{group}_wait")
        halves.update({n: _add_all(s, r, where, name=f"add_all_{n}") for n, s, r in zip(names, srcs, recv)})
    halves = [halves[n] for n in BIG]
    grads = {}
    for n, own, other in zip(BIG, halves, _share_halves(halves)):
        pair = jnp.stack([own, other])
        grads[n] = jnp.where(c == 0, pair, pair[::-1]).reshape(2 * own.shape[0], own.shape[1])

    packed = jnp.concatenate([
        small["norm_mix_g"], small["b_gate"].reshape(2, D_MODEL), small["norm_ffn_g"], small["norm_final_g"],
        jnp.pad(small["b_fgt"], ((0, 0), (0, D_MODEL - F_PAD))), jnp.pad(loss_part, ((0, 0), (0, D_MODEL - 1))),
        jnp.zeros((1, D_MODEL), F32)], axis=0)
    summed = _sum_small(packed)
    for n in SMALL:
        lo, hi = SMALL_ROWS[n]
        grads[n] = summed[lo:hi].reshape(1, -1)[:, :weights[n].size]
    loss = summed[6, 0]

    out_g, out_d, out_m, out_v = {}, {}, {}, {}
    for n in ORDER:
        shape = weights[n].shape
        two_d = shape[1:] if len(shape) == 3 else (1, weights[n].size)
        g2 = grads[n].reshape(two_d)
        wmv = adam_in if n == "w_in" else [t.reshape(two_d) for t in (weights[n], m_in[n], v_in[n])]
        d2, m2, v2 = _adamw(wmv[0], g2, wmv[1], wmv[2], name=f"adamw_{n}")
        out_g[n], out_d[n], out_m[n], out_v[n] = (g2.reshape(shape), d2.reshape(shape), m2.reshape(shape),
                                                  v2.reshape(shape))
    return (loss, grad_x[None], *[out_g[n] for n in ORDER], *[out_d[n] for n in ORDER],
            *[out_m[n] for n in ORDER], *[out_v[n] for n in ORDER])
```

```python
import numpy as np
import jax
import jax.numpy as jnp
from jax import lax
from jax.experimental import pallas as pl
from jax.experimental.pallas import tpu as pltpu

F32 = jnp.float32
_CD = jnp.bfloat16

D_MODEL = 1024
HEAD_DIM = 64
DIL_PAIRS = ((128, 1), (512, 4), (2048, 16))
N_DIL_GROUPS = 3
DIL_HEADS = 4
DIL_W = 128
DIL_OUT = DIL_HEADS * HEAD_DIM
DIL_WIDTH = N_DIL_GROUPS * DIL_OUT
N_FOX_HEADS = 8
FOX_WIDTH = N_FOX_HEADS * HEAD_DIM
D_FF = 2816
QKV_COLS = 3 * DIL_WIDTH + 3 * FOX_WIDTH
F_PAD = 128
RMS_EPS = 1e-6
NEG_INF = -1e30
ATTN_SCALE = HEAD_DIM ** -0.5
ADAM_LR, ADAM_B1, ADAM_B2, ADAM_EPS, ADAM_WD, ADAM_STEP = 0.001, 0.9, 0.999, 1e-08, 0.01, 10

VMEM_LIMIT = 48 * 1024 * 1024
VMEM_LIMIT_RESIDENT = 56 * 1024 * 1024
LANES = 128
MESH = pl.DeviceIdType.MESH
HBM_SPEC = pl.BlockSpec(memory_space=pltpu.HBM)


def _pcall(body, after=None, **kw):
    if after is None:
        return pl.pallas_call(body, **kw)
    n_in = len(kw["in_specs"])
    kw["in_specs"] = list(kw["in_specs"]) + [pl.BlockSpec(memory_space=pl.ANY)]

    def tied(*refs):
        return body(*refs[:n_in], *refs[n_in + 1:])

    call = pl.pallas_call(tied, **kw)
    return lambda *args: call(*args, after)


def _params(*sem):
    return pltpu.CompilerParams(dimension_semantics=sem, vmem_limit_bytes=VMEM_LIMIT)


def _pick(dim, pref):
    t = (min(pref, dim) // 128) * 128
    while t >= 128:
        if dim % t == 0:
            return t
        t -= 128
    return dim


def _mm(a, b, *, name, ta=False, tb=False, out_dtype=F32, add=None, tm=1024, tn=512, tk=2048, after=None,
        b_blocks=False, out_blocks=None, a_halves=False, b_halves=False):
    if a_halves:
        M, K = a.shape[1], 2 * a.shape[2]
    elif ta:
        K, M = a.shape
    else:
        M, K = a.shape
    if b_halves:
        b_rows, b_cols = b.shape[1], 2 * b.shape[2]
    else:
        b_rows, b_cols = (b.shape[1], b.shape[0] * b.shape[2]) if b_blocks else b.shape
    if tb:
        N, K2 = b_rows, b_cols
    else:
        K2, N = b_rows, b_cols
    assert K == K2, (a.shape, b.shape)
    shard = b.shape[2] if b_blocks else None
    tm = _pick(M, tm)
    tn = _pick(shard if (b_blocks and not tb) else (out_blocks or N), tn)
    tk = _pick(shard if (b_blocks and tb) else K, tk)
    nk = K // tk
    dn = (((0 if ta else 1,), (1 if tb else 0,)), ((), ()))
    has_add = add is not None
    assert not (has_add and out_blocks)

    def body(*refs):
        a_ref, b_ref = refs[0], refs[1]
        add_ref = refs[2] if has_add else None
        o_ref = refs[3] if has_add else refs[2]
        bv = b_ref[0] if b_blocks else b_ref[...]
        p = lax.dot_general(a_ref[...].astype(_CD), bv.astype(_CD), dn, preferred_element_type=F32)

        def finish(r):
            if has_add:
                r = r + add_ref[...]
            if out_blocks:
                o_ref[0] = r.astype(out_dtype)
            else:
                o_ref[...] = r.astype(out_dtype)

        if nk == 1:
            finish(p)
        else:
            acc_ref = refs[-1]
            k = pl.program_id(2)

            @pl.when(k == 0)
            def _():
                acc_ref[...] = p

            @pl.when(k > 0)
            def _():
                acc_ref[...] += p

            @pl.when(k == nk - 1)
            def _():
                finish(acc_ref[...])

    if a_halves:
        ka = (K // 2) // tk
        a_spec = pl.BlockSpec((None, tm, tk), lambda i, j, k: (k // ka, i, k % ka))
    else:
        a_spec = pl.BlockSpec((tk, tm), lambda i, j, k: (k, i)) if ta else pl.BlockSpec((tm, tk), lambda i, j, k: (i, k))
    if b_halves:
        nb_ = (N // 2) // tn
        b_spec = pl.BlockSpec((None, tk, tn), lambda i, j, k: (j // nb_, k, j % nb_))
    elif b_blocks and tb:
        per = shard // tk
        b_spec = pl.BlockSpec((1, tn, tk), lambda i, j, k: (k // per, j, k % per))
    elif b_blocks:
        per = shard // tn
        b_spec = pl.BlockSpec((1, tk, tn), lambda i, j, k: (j // per, k, j % per))
    else:
        b_spec = pl.BlockSpec((tn, tk), lambda i, j, k: (j, k)) if tb else pl.BlockSpec((tk, tn), lambda i, j, k: (k, j))
    if out_blocks:
        oper = out_blocks // tn
        o_spec = pl.BlockSpec((1, tm, tn), lambda i, j, k: (j // oper, i, j % oper))
        out_shape = jax.ShapeDtypeStruct((N // out_blocks, M, out_blocks), out_dtype)
    else:
        o_spec = pl.BlockSpec((tm, tn), lambda i, j, k: (i, j))
        out_shape = jax.ShapeDtypeStruct((M, N), out_dtype)
    in_specs = [a_spec, b_spec] + ([o_spec] if has_add else [])
    args = (a, b) + ((add,) if has_add else ())
    return _pcall(
        body, after, name=name, grid=(M // tm, N // tn, nk), in_specs=in_specs, out_specs=o_spec,
        out_shape=out_shape,
        scratch_shapes=[pltpu.VMEM((tm, tn), F32)] if nk > 1 else [],
        compiler_params=_params("parallel", "parallel", "arbitrary"),
    )(*args)


def _rms_fwd(x, g, *, name, tm=512, after=None):
    S, D = x.shape

    def body(x_ref, g_ref, h_ref):
        xv = x_ref[...]
        r = lax.rsqrt(jnp.mean(xv * xv, axis=-1, keepdims=True) + RMS_EPS)
        h_ref[...] = ((xv * r) * g_ref[...]).astype(h_ref.dtype)

    row = pl.BlockSpec((tm, D), lambda i: (i, 0))
    return _pcall(body, after, name=name, grid=(S // tm,), in_specs=[row, pl.BlockSpec((1, D), lambda i: (0, 0))],
                  out_specs=row, out_shape=jax.ShapeDtypeStruct((S, D), _CD), compiler_params=_params("parallel"))(x, g)


def _rms_bwd(x, g, dh, dres, *, name, tm=512, after=None):
    S, D = x.shape

    def body(x_ref, g_ref, dh_ref, dres_ref, dx_ref, dg_ref):
        xv = x_ref[...]
        r = lax.rsqrt(jnp.mean(xv * xv, axis=-1, keepdims=True) + RMS_EPS)
        xh = xv * r
        dhv = dh_ref[...]
        dxh = dhv * g_ref[...]
        dx_ref[...] = dres_ref[...] + r * (dxh - xh * jnp.mean(dxh * xh, axis=-1, keepdims=True))
        part = jnp.sum(dhv * xh, axis=0, keepdims=True)

        @pl.when(pl.program_id(0) == 0)
        def _():
            dg_ref[...] = part

        @pl.when(pl.program_id(0) > 0)
        def _():
            dg_ref[...] += part

    row = pl.BlockSpec((tm, D), lambda i: (i, 0))
    vec = pl.BlockSpec((1, D), lambda i: (0, 0))
    return _pcall(body, after, name=name, grid=(S // tm,), in_specs=[row, vec, row, row], out_specs=[row, vec],
                  out_shape=[jax.ShapeDtypeStruct((S, D), F32), jax.ShapeDtypeStruct((1, D), F32)],
                  compiler_params=_params("arbitrary"))(x, g, dh, dres)


def _loss_head(x, g, tgt, *, name, tm=512):
    S, D = x.shape

    def body(x_ref, g_ref, t_ref, loss_ref, dx_ref, dg_ref):
        xv = x_ref[...]
        gv = g_ref[...]
        r = lax.rsqrt(jnp.mean(xv * xv, axis=-1, keepdims=True) + RMS_EPS)
        xh = xv * r
        err = xh * gv - t_ref[...]
        lpart = 0.5 * jnp.sum(jnp.mean(err * err, axis=-1, keepdims=True), axis=0, keepdims=True)
        dy = err * (1.0 / D)
        dxh = dy * gv
        dx_ref[...] = r * (dxh - xh * jnp.mean(dxh * xh, axis=-1, keepdims=True))
        gpart = jnp.sum(dy * xh, axis=0, keepdims=True)

        @pl.when(pl.program_id(0) == 0)
        def _():
            loss_ref[...] = lpart
            dg_ref[...] = gpart

        @pl.when(pl.program_id(0) > 0)
        def _():
            loss_ref[...] += lpart
            dg_ref[...] += gpart

    row = pl.BlockSpec((tm, D), lambda i: (i, 0))
    vec = pl.BlockSpec((1, D), lambda i: (0, 0))
    one = pl.BlockSpec((1, 1), lambda i: (0, 0))
    return _pcall(body, name=name, grid=(S // tm,), in_specs=[row, vec, row], out_specs=[one, row, vec],
                  out_shape=[jax.ShapeDtypeStruct((1, 1), F32), jax.ShapeDtypeStruct((S, D), F32),
                             jax.ShapeDtypeStruct((1, D), F32)],
                  compiler_params=_params("arbitrary"))(x, g, tgt)


def _sigmoid(z):
    return 1.0 / (1.0 + jnp.exp(-z))


def _gate_fwd(gl, bg, ya, yb, *, name, tm=512):
    S, D = ya.shape

    def body(za_ref, zb_ref, ba_ref, bb_ref, ya_ref, yb_ref, o_ref):
        ga = _sigmoid(za_ref[...].astype(F32) + ba_ref[...])
        gb = _sigmoid(zb_ref[...].astype(F32) + bb_ref[...])
        o_ref[...] = (ga * ya_ref[...].astype(F32) + gb * yb_ref[...].astype(F32)).astype(o_ref.dtype)

    lo = pl.BlockSpec((tm, D), lambda i: (i, 0))
    hi = pl.BlockSpec((tm, D), lambda i: (i, 1))
    vlo = pl.BlockSpec((1, D), lambda i: (0, 0))
    vhi = pl.BlockSpec((1, D), lambda i: (0, 1))
    return _pcall(body, name=name, grid=(S // tm,), in_specs=[lo, hi, vlo, vhi, lo, lo], out_specs=lo,
                  out_shape=jax.ShapeDtypeStruct((S, D), _CD), compiler_params=_params("parallel"))(gl, gl, bg, bg, ya, yb)


def _gate_bwd(dm, gl, bg, ya, yb, *, name, tm=256):
    S, D = ya.shape

    def body(dm_ref, za_ref, zb_ref, ba_ref, bb_ref, ya_ref, yb_ref, dya_ref, dyb_ref, dgl_ref, dbg_ref):
        dmv = dm_ref[...].astype(F32)
        ga = _sigmoid(za_ref[...].astype(F32) + ba_ref[...])
        gb = _sigmoid(zb_ref[...].astype(F32) + bb_ref[...])
        dya_ref[...] = (dmv * ga).astype(dya_ref.dtype)
        dyb_ref[...] = (dmv * gb).astype(dyb_ref.dtype)
        dza = dmv * ya_ref[...].astype(F32) * ga * (1.0 - ga)
        dzb = dmv * yb_ref[...].astype(F32) * gb * (1.0 - gb)
        dgl_ref[:, :D] = dza.astype(dgl_ref.dtype)
        dgl_ref[:, D:] = dzb.astype(dgl_ref.dtype)
        pa = jnp.sum(dza, axis=0, keepdims=True)
        pb = jnp.sum(dzb, axis=0, keepdims=True)

        @pl.when(pl.program_id(0) == 0)
        def _():
            dbg_ref[:, :D] = pa
            dbg_ref[:, D:] = pb

        @pl.when(pl.program_id(0) > 0)
        def _():
            dbg_ref[:, :D] += pa
            dbg_ref[:, D:] += pb

    lo = pl.BlockSpec((tm, D), lambda i: (i, 0))
    hi = pl.BlockSpec((tm, D), lambda i: (i, 1))
    vlo = pl.BlockSpec((1, D), lambda i: (0, 0))
    vhi = pl.BlockSpec((1, D), lambda i: (0, 1))
    wide = pl.BlockSpec((tm, 2 * D), lambda i: (i, 0))
    vwide = pl.BlockSpec((1, 2 * D), lambda i: (0, 0))
    return _pcall(body, name=name, grid=(S // tm,), in_specs=[lo, lo, hi, vlo, vhi, lo, lo],
                  out_specs=[lo, lo, wide, vwide],
                  out_shape=[jax.ShapeDtypeStruct((S, D), _CD), jax.ShapeDtypeStruct((S, D), _CD),
                             jax.ShapeDtypeStruct((S, 2 * D), _CD), jax.ShapeDtypeStruct((1, 2 * D), F32)],
                  compiler_params=_params("arbitrary"))(dm, gl, gl, bg, bg, ya, yb)


def _ffn_in_act(h2, w_blocks, *, name, tm=512):
    S, D = h2.shape
    _, _, C = w_blocks.shape

    def body(a_ref, bg_ref, bu_ref, g_ref, u_ref, o_ref):
        av = a_ref[...].astype(_CD)
        gv = jnp.dot(av, bg_ref[0].astype(_CD), preferred_element_type=F32)
        uv = jnp.dot(av, bu_ref[0].astype(_CD), preferred_element_type=F32)
        g_ref[...] = gv.astype(g_ref.dtype)
        u_ref[...] = uv.astype(u_ref.dtype)
        o_ref[...] = (gv * _sigmoid(gv) * uv).astype(o_ref.dtype)

    out = pl.BlockSpec((tm, C), lambda i, j: (i, j))
    shp = jax.ShapeDtypeStruct((S, 2 * C), _CD)
    return _pcall(body, name=name, grid=(S // tm, 2),
                  in_specs=[pl.BlockSpec((tm, D), lambda i, j: (i, 0)), pl.BlockSpec((1, D, C), lambda i, j: (j, 0, 0)),
                            pl.BlockSpec((1, D, C), lambda i, j: (2 + j, 0, 0))],
                  out_specs=[out, out, out], out_shape=[shp, shp, shp],
                  compiler_params=_params("parallel", "arbitrary"))(h2, w_blocks, w_blocks)


def _d_swiglu(dx, w_down, gate, up, *, name, tm=512, tn=1408):
    S, D = dx.shape
    F = w_down.shape[0]
    nt = (((1,), (1,)), ((), ()))

    def body(a_ref, b_ref, g_ref, u_ref, o_ref):
        dv = lax.dot_general(a_ref[...].astype(_CD), b_ref[...].astype(_CD), nt, preferred_element_type=F32)
        gv = g_ref[...].astype(F32)
        sg = _sigmoid(gv)
        o_ref[0] = (dv * u_ref[...].astype(F32) * (sg * (1.0 + gv * (1.0 - sg)))).astype(o_ref.dtype)
        o_ref[1] = (dv * (gv * sg)).astype(o_ref.dtype)

    tile = pl.BlockSpec((tm, tn), lambda i, j: (i, j))
    return _pcall(body, name=name, grid=(S // tm, F // tn),
                  in_specs=[pl.BlockSpec((tm, D), lambda i, j: (i, 0)), pl.BlockSpec((tn, D), lambda i, j: (j, 0)),
                            tile, tile],
                  out_specs=pl.BlockSpec((2, tm, tn), lambda i, j: (0, i, j)),
                  out_shape=jax.ShapeDtypeStruct((2, S, F), _CD),
                  compiler_params=_params("parallel", "arbitrary"))(dx, w_down, gate, up)


def _split3(x):
    hi = x.astype(jnp.bfloat16)
    r1 = x - hi.astype(F32)
    mid = r1.astype(jnp.bfloat16)
    lo = (r1 - mid.astype(F32)).astype(jnp.bfloat16)
    return hi, mid, lo


def _ones_dot_left(ones, x):
    return sum(jnp.dot(ones, p, preferred_element_type=F32) for p in _split3(x))


def _ones_dot_right(x, ones):
    return sum(jnp.dot(p, ones, preferred_element_type=F32) for p in _split3(x))


def _head_sum(x):
    n = x.shape[1]
    r = lax.broadcasted_iota(jnp.int32, (n, n), 0) // HEAD_DIM
    c = lax.broadcasted_iota(jnp.int32, (n, n), 1) // HEAD_DIM
    return _ones_dot_right(x, (r == c).astype(jnp.bfloat16))


def _log_sigmoid(z):
    e = jnp.exp(-jnp.abs(z))
    t = 1.0 + e
    log1p_e = jnp.where(t == 1.0, e, jnp.log(t) * (e / jnp.where(t == 1.0, 1.0, t - 1.0)))
    return jnp.minimum(z, 0.0) - log1p_e


def _fox_cumsum(zf, bf, *, name):
    S, W = zf.shape
    nb = S // 128

    def body(z_ref, b_ref, c_ref):
        tri = (lax.broadcasted_iota(jnp.int32, (128, 128), 0) >= lax.broadcasted_iota(jnp.int32, (128, 128), 1))
        tri = tri.astype(jnp.bfloat16)

        def step(i, carry):
            rows = pl.ds(pl.multiple_of(i * 128, 128), 128)
            lf = _log_sigmoid(z_ref[rows, :] + b_ref[...])
            cb = _ones_dot_left(tri, lf) + carry
            c_ref[rows, :] = cb
            return cb[127:128, :]

        lax.fori_loop(0, nb, step, jnp.zeros((1, W), F32))

    return _pcall(body, name=name, out_shape=jax.ShapeDtypeStruct((S, W), F32),
                  compiler_params=pltpu.CompilerParams(vmem_limit_bytes=VMEM_LIMIT))(zf, bf)


def _fox_cumsum_bwd(dc, zf, bf, *, name):
    S, W = zf.shape
    nb = S // 128

    def body(dc_ref, z_ref, b_ref, dz_ref, db_ref):
        tri = (lax.broadcasted_iota(jnp.int32, (128, 128), 0) <= lax.broadcasted_iota(jnp.int32, (128, 128), 1))
        tri = tri.astype(jnp.bfloat16)

        def step(k, carry):
            tail, acc = carry
            i = nb - 1 - k
            rows = pl.ds(pl.multiple_of(i * 128, 128), 128)
            dlf = _ones_dot_left(tri, dc_ref[rows, :]) + tail
            dz = dlf * _sigmoid(-(z_ref[rows, :] + b_ref[...]))
            dz_ref[rows, :] = dz
            return dlf[0:1, :], acc + jnp.sum(dz, axis=0, keepdims=True)

        _, acc = lax.fori_loop(0, nb, step, (jnp.zeros((1, W), F32), jnp.zeros((1, W), F32)))
        db_ref[...] = acc

    return _pcall(body, name=name,
                  out_shape=[jax.ShapeDtypeStruct((S, W), F32), jax.ShapeDtypeStruct((1, W), F32)],
                  compiler_params=pltpu.CompilerParams(vmem_limit_bytes=VMEM_LIMIT))(dc, zf, bf)


def _proj_dil(h, w_qkv, *, name, tm=1024):
    S, D = h.shape
    tn = DIL_WIDTH

    def body(a_ref, b_ref, *rest):
        outs, acc = rest[:N_DIL_GROUPS], rest[N_DIL_GROUPS]
        prod = jnp.dot(a_ref[...].astype(_CD), b_ref[...].astype(_CD), preferred_element_type=F32)
        for k in range(tn // LANES):
            acc[k] = prod[:, k * LANES:(k + 1) * LANES]
        for g, (_, d) in enumerate(DIL_PAIRS):
            for half in range(DIL_OUT // LANES):
                k = g * (DIL_OUT // LANES) + half
                cols = slice(half * LANES, (half + 1) * LANES)
                for r in range(d):
                    rows = pl.ds(r, tm // d, stride=d) if d > 1 else slice(None)
                    outs[g][0, r, :, cols] = acc[k, rows, :].astype(outs[g].dtype)

    out_specs = [pl.BlockSpec((1, d, tm // d, DIL_OUT), lambda i, j: (j, 0, i, 0)) for _, d in DIL_PAIRS]
    out_shape = [jax.ShapeDtypeStruct((3, d, S // d, DIL_OUT), _CD) for _, d in DIL_PAIRS]
    outs = _pcall(body, name=name, grid=(S // tm, 3),
                  in_specs=[pl.BlockSpec((tm, D), lambda i, j: (i, 0)), pl.BlockSpec((D, tn), lambda i, j: (0, j))],
                  out_specs=out_specs, out_shape=out_shape, scratch_shapes=[pltpu.VMEM((tn // LANES, tm, LANES), F32)],
                  compiler_params=_params("parallel", "arbitrary"))(h, w_qkv)
    return [o.reshape(3, S, DIL_OUT) for o in outs]


def _dil_start(block, S, dilation):
    sub = S // dilation
    u0 = block * DIL_W
    return (u0 % sub) * dilation + u0 // sub


def _dil_slopes(group):
    h = np.arange(1, N_DIL_GROUPS * DIL_HEADS + 1, dtype=np.float32)
    s = (np.float32(2.0) ** (np.float32(-8.0) * h / np.float32(N_DIL_GROUPS * DIL_HEADS))).astype(np.float32)
    return [float(v) for v in s.reshape(N_DIL_GROUPS, DIL_HEADS)[group]]


def _dil_tiles(i, n, blocks_per_seq):
    qi = lax.broadcasted_iota(jnp.int32, (DIL_W, 2 * DIL_W), 0)
    kj = lax.broadcasted_iota(jnp.int32, (DIL_W, 2 * DIL_W), 1)
    rel = qi + DIL_W - kj
    first = ((4 * n + i) % blocks_per_seq) == 0
    valid = jnp.logical_and(jnp.logical_and(rel >= 0, rel <= DIL_W), jnp.logical_or(kj >= DIL_W, jnp.logical_not(first)))
    return valid, rel.astype(F32)


def _dil_window(cur_ref, prev_ref, i, cols):
    if i > 0:
        return cur_ref[(i - 1) * DIL_W:(i + 1) * DIL_W, cols]
    return jnp.concatenate([prev_ref[:, cols], cur_ref[:DIL_W, cols]], axis=0)


CHUNK = 4 * DIL_W


def _dil_rows(block, S, dilation):
    start = _dil_start(block, S, dilation)
    return pl.ds(start, DIL_W, stride=dilation) if dilation > 1 else pl.ds(start, DIL_W)


def SPLIT(S):
    return (DIL_OUT // LANES, S, LANES)


def _dil_fwd(qkv, group, *, name):
    S = qkv.shape[1]
    dilation = DIL_PAIRS[group][1]
    bps = (S // dilation) // DIL_W
    slopes = _dil_slopes(group)
    nt = (((1,), (1,)), ((), ()))

    def body(q_ref, k_ref, v_ref, kp_ref, vp_ref, on_ref, ln_ref, o_ref, l_ref):
        n = pl.program_id(0)
        for i in range(4):
            valid, rel = _dil_tiles(i, n, bps)
            rows = slice(i * DIL_W, (i + 1) * DIL_W)
            for h in range(DIL_HEADS):
                cols = slice(h * HEAD_DIM, (h + 1) * HEAD_DIM)
                qh = q_ref[rows, cols]
                k2, v2 = _dil_window(k_ref, kp_ref, i, cols), _dil_window(v_ref, vp_ref, i, cols)
                s = lax.dot_general(qh, k2, nt, preferred_element_type=F32) * ATTN_SCALE - (slopes[h] * dilation) * rel
                s = jnp.where(valid, s, NEG_INF)
                m = jnp.max(s, axis=-1, keepdims=True)
                p = jnp.exp(s - m)
                den = jnp.sum(p, axis=-1, keepdims=True)
                acc = jnp.dot(p.astype(_CD), v2, preferred_element_type=F32)
                o_ref[rows, cols] = acc / den
                l_ref[rows, cols] = jnp.broadcast_to(m + jnp.log(den), (DIL_W, HEAD_DIM))
        for i in range(4):
            rows = slice(i * DIL_W, (i + 1) * DIL_W)
            nat = _dil_rows(4 * n + i, S, dilation)
            for half in range(DIL_OUT // LANES):
                cols = slice(half * LANES, (half + 1) * LANES)
                on_ref[half, nat, :] = o_ref[rows, cols]
                ln_ref[half, nat, :] = l_ref[rows, cols]

    def cur(which):
        return pl.BlockSpec((None, CHUNK, DIL_OUT), lambda n: (which, n, 0))

    def prev(which):
        return pl.BlockSpec((None, DIL_W, DIL_OUT), lambda n: (which, jnp.maximum(4 * n - 1, 0), 0))

    whole = pl.BlockSpec(SPLIT(S), lambda n: (0, 0, 0))
    return _pcall(body, name=name, grid=(S // CHUNK,), in_specs=[cur(0), cur(1), cur(2), prev(1), prev(2)],
                  out_specs=[whole, whole],
                  out_shape=[jax.ShapeDtypeStruct(SPLIT(S), F32), jax.ShapeDtypeStruct(SPLIT(S), F32)],
                  scratch_shapes=[pltpu.VMEM((CHUNK, DIL_OUT), F32), pltpu.VMEM((CHUNK, DIL_OUT), F32)],
                  compiler_params=_params("arbitrary"))(qkv, qkv, qkv, qkv, qkv)


STAT_OFFSET = HEAD_DIM // 2


def _dil_bwd(qkv, stats, do, group, *, name):
    S = qkv.shape[1]
    dilation = DIL_PAIRS[group][1]
    bps = (S // dilation) // DIL_W
    slopes = _dil_slopes(group)
    nchunk = S // CHUNK
    nt = (((1,), (1,)), ((), ()))
    tn = (((0,), (0,)), ((), ()))

    def body(q_ref, k_ref, v_ref, kp_ref, vp_ref, ln_ref, don_ref, dqn_ref, dkn_ref, dvn_ref,
             dk_s, dv_s, l_ref, do_ref, dq_ref):
        step = pl.program_id(0)
        n = nchunk - 1 - step
        for i in range(4):
            rows = slice(i * DIL_W, (i + 1) * DIL_W)
            nat = _dil_rows(4 * n + i, S, dilation)
            for half in range(DIL_OUT // LANES):
                cols = slice(half * LANES, (half + 1) * LANES)
                l_ref[rows, cols] = ln_ref[half, nat, :]
                do_ref[rows, cols] = don_ref[half, nat, :]

        @pl.when(step == 0)
        def _():
            dk_s[:, CHUNK:] = jnp.zeros((DIL_OUT, DIL_W), F32)
            dv_s[:, CHUNK:] = jnp.zeros((DIL_OUT, DIL_W), F32)

        dk_s[:, :CHUNK] = jnp.zeros((DIL_OUT, CHUNK), F32)
        dv_s[:, :CHUNK] = jnp.zeros((DIL_OUT, CHUNK), F32)
        for i in range(4):
            valid, rel = _dil_tiles(i, n, bps)
            rows = slice(i * DIL_W, (i + 1) * DIL_W)
            window = slice(i * DIL_W, (i + 2) * DIL_W)
            for h in range(DIL_HEADS):
                cols = slice(h * HEAD_DIM, (h + 1) * HEAD_DIM)
                qh = q_ref[rows, cols]
                k2, v2 = _dil_window(k_ref, kp_ref, i, cols), _dil_window(v_ref, vp_ref, i, cols)
                lh = l_ref[rows, h * HEAD_DIM:h * HEAD_DIM + 1]
                shift = l_ref[rows, h * HEAD_DIM + STAT_OFFSET:h * HEAD_DIM + STAT_OFFSET + 1]
                s = lax.dot_general(qh, k2, nt, preferred_element_type=F32) * ATTN_SCALE - (slopes[h] * dilation) * rel
                p = jnp.exp(jnp.where(valid, s, NEG_INF) - lh)
                dob = do_ref[rows, cols].astype(_CD)
                ds = p * (lax.dot_general(dob, v2, nt, preferred_element_type=F32) + shift)
                dsb = (ds * ATTN_SCALE).astype(_CD)
                dq_ref[rows, cols] = jnp.dot(dsb, k2, preferred_element_type=F32)
                dk_s[cols, window] += lax.dot_general(qh, dsb, tn, preferred_element_type=F32)
                dv_s[cols, window] += lax.dot_general(dob, p.astype(_CD), tn, preferred_element_type=F32)
        for i in range(4):
            rows = slice(i * DIL_W, (i + 1) * DIL_W)
            done = slice((i + 1) * DIL_W, (i + 2) * DIL_W)
            nat = _dil_rows(4 * n + i, S, dilation)
            dkb, dvb = dk_s[:, done].T, dv_s[:, done].T
            for half in range(DIL_OUT // LANES):
                cols = slice(half * LANES, (half + 1) * LANES)
                dqn_ref[half, nat, :] = dq_ref[rows, cols]
                dkn_ref[half, nat, :] = dkb[:, cols]
                dvn_ref[half, nat, :] = dvb[:, cols]
        dk_s[:, CHUNK:] = dk_s[:, :DIL_W]
        dv_s[:, CHUNK:] = dv_s[:, :DIL_W]

    def cur(which):
        return pl.BlockSpec((None, CHUNK, DIL_OUT), lambda s: (which, nchunk - 1 - s, 0))

    def prev(which):
        return pl.BlockSpec((None, DIL_W, DIL_OUT), lambda s: (which, jnp.maximum(4 * (nchunk - 1 - s) - 1, 0), 0))

    whole = pl.BlockSpec(SPLIT(S), lambda s: (0, 0, 0))
    shp = jax.ShapeDtypeStruct(SPLIT(S), F32)
    tile = pltpu.VMEM((CHUNK, DIL_OUT), F32)
    return _pcall(body, name=name, grid=(nchunk,),
                  in_specs=[cur(0), cur(1), cur(2), prev(1), prev(2), whole, whole],
                  out_specs=[whole, whole, whole], out_shape=[shp, shp, shp],
                  scratch_shapes=[pltpu.VMEM((DIL_OUT, CHUNK + DIL_W), F32), pltpu.VMEM((DIL_OUT, CHUNK + DIL_W), F32),
                                  tile, tile, tile],
                  compiler_params=pltpu.CompilerParams(dimension_semantics=("arbitrary",),
                                                       vmem_limit_bytes=VMEM_LIMIT_RESIDENT))(
        qkv, qkv, qkv, qkv, qkv, stats, do)


def _dil_mix_fwd(os_, ls_, *, name, tm=512):
    nh, S, _ = os_[0].shape

    def body(o0, o1, o2, l0, l1, l2, out_ref):
        for half in range(nh):
            ls = [l0[half], l1[half], l2[half]]
            m = jnp.maximum(jnp.maximum(ls[0], ls[1]), ls[2])
            es = [jnp.exp(l - m) for l in ls]
            den = es[0] + es[1] + es[2]
            mixed = (es[0] * o0[half] + es[1] * o1[half] + es[2] * o2[half]) / den
            out_ref[:, half * LANES:(half + 1) * LANES] = mixed.astype(out_ref.dtype)

    halves = pl.BlockSpec((nh, tm, LANES), lambda i: (0, i, 0))
    row = pl.BlockSpec((tm, nh * LANES), lambda i: (i, 0))
    return _pcall(body, name=name, grid=(S // tm,), in_specs=[halves] * 6, out_specs=row,
                  out_shape=jax.ShapeDtypeStruct((S, nh * LANES), _CD), compiler_params=_params("parallel"))(*os_, *ls_)


def _dil_mix_bwd(doa, os_, ls_, *, name, tm=512, after=None):
    nh, S, _ = os_[0].shape

    def body(d_ref, o0, o1, o2, l0, l1, l2, do0, do1, do2, st0, st1, st2):
        first = lax.broadcasted_iota(jnp.int32, (tm, LANES), 1) % HEAD_DIM < STAT_OFFSET
        for half in range(nh):
            dv = d_ref[:, half * LANES:(half + 1) * LANES]
            ls = [l0[half], l1[half], l2[half]]
            m = jnp.maximum(jnp.maximum(ls[0], ls[1]), ls[2])
            es = [jnp.exp(l - m) for l in ls]
            den = es[0] + es[1] + es[2]
            al = [e / den for e in es]
            da = [_head_sum(dv * o[half]) for o in (o0, o1, o2)]
            mean = al[0] * da[0] + al[1] * da[1] + al[2] * da[2]
            for a, l, do_ref, st_ref in zip(al, ls, (do0, do1, do2), (st0, st1, st2)):
                do_ref[half] = a * dv
                st_ref[half] = jnp.where(first, l, -a * mean)

    halves = pl.BlockSpec((nh, tm, LANES), lambda i: (0, i, 0))
    row = pl.BlockSpec((tm, nh * LANES), lambda i: (i, 0))
    shp = jax.ShapeDtypeStruct((nh, S, LANES), F32)
    return _pcall(body, after, name=name, grid=(S // tm,), in_specs=[row] + [halves] * 6, out_specs=[halves] * 6,
                  out_shape=[shp] * 6, compiler_params=_params("parallel"))(doa, *os_, *ls_)


FOX_T = 512


PACK = 2 * HEAD_DIM
HEAD_PAIRS = N_FOX_HEADS // 2
FOX_HPS = 8
Q_BLOCK0 = 0
K_BLOCK0 = FOX_WIDTH // PACK
V_BLOCK0 = 2 * FOX_WIDTH // PACK


def _pieces(x):
    hi = x.astype(jnp.bfloat16).astype(F32)
    r = x - hi
    mid = r.astype(jnp.bfloat16).astype(F32)
    lo = (r - mid).astype(jnp.bfloat16).astype(F32)
    return [hi, mid, lo]


def _extras(first, second, rows):
    lane = lax.broadcasted_iota(jnp.int32, (rows, HEAD_DIM), 1)
    out = jnp.zeros((rows, HEAD_DIM), F32)
    for base, triple in ((0, first), (3, second)):
        if all(isinstance(v, float) for v in triple) and len(set(triple)) == 1:
            if triple[0] != 0.0:
                out = jnp.where(jnp.logical_and(lane >= base, lane < base + 3), triple[0], out)
        else:
            for idx, val in enumerate(triple):
                out = jnp.where(lane == base + idx, val, out)
    return out


def _head_column(c, h):
    lane = lax.broadcasted_iota(jnp.int32, c.shape, 1)
    return jnp.sum(jnp.where(lane == h, c, 0.0), axis=1, keepdims=True)


ONES3 = [1.0, 1.0, 1.0]
ZEROS3 = [0.0, 0.0, 0.0]


def _fox_pack_fwd(qkv, c, *, name, tm=512):
    S = qkv.shape[0]

    def body(q_ref, k_ref, v_ref, c_ref, qo_ref, ko_ref, vo_ref):
        hp = pl.program_id(1)
        cv = c_ref[...]
        v_extras = jnp.where(lax.broadcasted_iota(jnp.int32, (tm, HEAD_DIM), 1) < 3, 1.0, 0.0).astype(vo_ref.dtype)
        for hh in range(2):
            ch = _pieces(_head_column(cv, 2 * hp + hh))
            src = slice(hh * HEAD_DIM, (hh + 1) * HEAD_DIM)
            lo = slice(hh * PACK, hh * PACK + HEAD_DIM)
            hi = slice(hh * PACK + HEAD_DIM, (hh + 1) * PACK)
            qo_ref[:, lo] = (q_ref[:, src].astype(F32) * ATTN_SCALE).astype(qo_ref.dtype)
            qo_ref[:, hi] = _extras(ch, ONES3, tm).astype(qo_ref.dtype)
            ko_ref[:, lo] = k_ref[:, src]
            ko_ref[:, hi] = _extras(ONES3, [-p for p in ch], tm).astype(ko_ref.dtype)
            vo_ref[:, lo] = v_ref[:, src]
            vo_ref[:, hi] = v_extras

    def src(block0):
        return pl.BlockSpec((tm, PACK), lambda i, hp: (i, block0 + hp))

    out = pl.BlockSpec((tm, 2 * PACK), lambda i, hp: (i, hp))
    shp = jax.ShapeDtypeStruct((S, N_FOX_HEADS * PACK), _CD)
    return _pcall(body, name=name, grid=(S // tm, HEAD_PAIRS),
                  in_specs=[src(Q_BLOCK0), src(K_BLOCK0), src(V_BLOCK0), pl.BlockSpec((tm, PACK), lambda i, hp: (i, 0))],
                  out_specs=[out, out, out], out_shape=[shp, shp, shp],
                  compiler_params=_params("parallel", "parallel"))(qkv, qkv, qkv, c)


def _fox_fwd(qp, kp, vp, *, name):
    S = qp.shape[0]
    nt = S // FOX_T
    nt_dims = (((1,), (1,)), ((), ()))
    tn_dims = (((0,), (0,)), ((), ()))

    def body(i_tab, j_tab, q_ref, k_ref, v_ref, o_ref, l_ref, m_s, acc_s):
        t = pl.program_id(1)
        i, j = i_tab[t], j_tab[t]

        @pl.when(j == 0)
        def _():
            m_s[...] = jnp.full((FOX_HPS, 1, FOX_T), NEG_INF, F32)
            acc_s[...] = jnp.zeros((FOX_HPS, PACK, FOX_T), F32)

        def tile(diagonal):
            for hh in range(FOX_HPS):
                cols = slice(hh * PACK, (hh + 1) * PACK)
                st = lax.dot_general(k_ref[:, cols], q_ref[:, cols], nt_dims, preferred_element_type=F32)
                if diagonal:
                    key = lax.broadcasted_iota(jnp.int32, (FOX_T, FOX_T), 0)
                    qry = lax.broadcasted_iota(jnp.int32, (FOX_T, FOX_T), 1)
                    st = jnp.where(key <= qry, st, NEG_INF)
                m_old = m_s[hh]
                m_new = jnp.maximum(m_old, jnp.max(st, axis=0, keepdims=True))
                pt = jnp.exp(st - m_new)
                acc_s[hh] = jnp.exp(m_old - m_new) * acc_s[hh] + lax.dot_general(
                    v_ref[:, cols], pt.astype(_CD), tn_dims, preferred_element_type=F32)
                m_s[hh] = m_new

        @pl.when(j < i)
        def _():
            tile(False)

        @pl.when(j == i)
        def _():
            tile(True)
            for hh in range(FOX_HPS):
                acc = acc_s[hh]
                den = acc[HEAD_DIM:HEAD_DIM + 1, :]
                cols = slice(hh * HEAD_DIM, (hh + 1) * HEAD_DIM)
                o_ref[:, cols] = (acc[:HEAD_DIM, :] / den).T
                l_ref[:, cols] = jnp.broadcast_to(m_s[hh] + jnp.log(den), (HEAD_DIM, FOX_T)).T

    pairs = [(i, j) for i in range(nt) for j in range(i + 1)]
    i_tab = jnp.asarray([p[0] for p in pairs], jnp.int32)
    j_tab = jnp.asarray([p[1] for p in pairs], jnp.int32)
    qs = pl.BlockSpec((FOX_T, FOX_HPS * PACK), lambda hp, t, it, jt: (it[t], hp))
    ks = pl.BlockSpec((FOX_T, FOX_HPS * PACK), lambda hp, t, it, jt: (jt[t], hp))
    os_ = pl.BlockSpec((FOX_T, FOX_HPS * HEAD_DIM), lambda hp, t, it, jt: (it[t], hp))
    shp = jax.ShapeDtypeStruct((S, FOX_WIDTH), F32)
    grid_spec = pltpu.PrefetchScalarGridSpec(
        num_scalar_prefetch=2, grid=(N_FOX_HEADS // FOX_HPS, len(pairs)), in_specs=[qs, ks, ks], out_specs=[os_, os_],
        scratch_shapes=[pltpu.VMEM((FOX_HPS, 1, FOX_T), F32), pltpu.VMEM((FOX_HPS, PACK, FOX_T), F32)])
    return _pcall(body, name=name, grid_spec=grid_spec, out_shape=[shp, shp],
                  compiler_params=_params("parallel", "arbitrary"))(i_tab, j_tab, qp, kp, vp)


def _fox_pack_bwd(qkv, c, o, lse, do, *, name, tm=512, after=None):
    S = qkv.shape[0]

    def body(q_ref, c_ref, o_ref, l_ref, do_ref, qo_ref, do_out_ref):
        hp = pl.program_id(1)
        cv = c_ref[...]
        for hh in range(2):
            src = slice(hh * HEAD_DIM, (hh + 1) * HEAD_DIM)
            lo = slice(hh * PACK, hh * PACK + HEAD_DIM)
            hi = slice(hh * PACK + HEAD_DIM, (hh + 1) * PACK)
            shift = _head_column(cv, 2 * hp + hh) - l_ref[:, hh * HEAD_DIM:hh * HEAD_DIM + 1]
            dov = do_ref[:, src]
            dsum = jnp.sum(dov * o_ref[:, src], axis=-1, keepdims=True)
            qo_ref[:, lo] = (q_ref[:, src].astype(F32) * ATTN_SCALE).astype(qo_ref.dtype)
            qo_ref[:, hi] = _extras(_pieces(shift), ONES3, tm).astype(qo_ref.dtype)
            do_out_ref[:, lo] = dov.astype(do_out_ref.dtype)
            do_out_ref[:, hi] = _extras(_pieces(-dsum), ZEROS3, tm).astype(do_out_ref.dtype)

    pair = pl.BlockSpec((tm, PACK), lambda i, hp: (i, hp))
    out = pl.BlockSpec((tm, 2 * PACK), lambda i, hp: (i, hp))
    shp = jax.ShapeDtypeStruct((S, N_FOX_HEADS * PACK), _CD)
    return _pcall(body, after, name=name, grid=(S // tm, HEAD_PAIRS),
                  in_specs=[pl.BlockSpec((tm, PACK), lambda i, hp: (i, Q_BLOCK0 + hp)),
                            pl.BlockSpec((tm, PACK), lambda i, hp: (i, 0)), pair, pair, pair],
                  out_specs=[out, out], out_shape=[shp, shp],
                  compiler_params=_params("parallel", "parallel"))(qkv, c, o, lse, do)


def _fox_bwd(qp, kp, vp, dop, *, name):
    S = qp.shape[0]
    nt = S // FOX_T
    nt_dims = (((1,), (1,)), ((), ()))
    tn_dims = (((0,), (0,)), ((), ()))

    def body(i_tab, j_tab, q_ref, k_ref, v_ref, do_ref, dq_ref, dk_ref, dv_ref, dc_ref, dr_ref,
             dq_s, dk_s, dv_s, dc_s, dr_s):
        t = pl.program_id(1)
        i, j = i_tab[t], j_tab[t]

        @pl.when(t == 0)
        def _():
            dq_s[...] = jnp.zeros((S, FOX_HPS * PACK), F32)
            dr_s[...] = jnp.zeros((FOX_HPS, 1, S), F32)

        @pl.when(i == j)
        def _():
            dk_s[...] = jnp.zeros((FOX_T, FOX_HPS * PACK), F32)
            dv_s[...] = jnp.zeros((FOX_T, FOX_HPS * PACK), F32)
            dc_s[...] = jnp.zeros((FOX_HPS, FOX_T, 1), F32)

        def tile(diagonal):
            rows = pl.ds(pl.multiple_of(i * FOX_T, FOX_T), FOX_T)
            for hh in range(FOX_HPS):
                cols = slice(hh * PACK, (hh + 1) * PACK)
                qv, kv, vv, dov = q_ref[:, cols], k_ref[:, cols], v_ref[:, cols], do_ref[:, cols]
                pt = jnp.exp(lax.dot_general(kv, qv, nt_dims, preferred_element_type=F32))
                if diagonal:
                    key = lax.broadcasted_iota(jnp.int32, (FOX_T, FOX_T), 0)
                    qry = lax.broadcasted_iota(jnp.int32, (FOX_T, FOX_T), 1)
                    pt = jnp.where(key <= qry, pt, 0.0)
                dst = pt * lax.dot_general(vv, dov, nt_dims, preferred_element_type=F32)
                dsb = dst.astype(_CD)
                dc_s[hh] += jnp.sum(dst, axis=1, keepdims=True)
                dr_s[hh, :, rows] += jnp.sum(dst, axis=0, keepdims=True)
                dv_s[:, cols] += jnp.dot(pt.astype(_CD), dov, preferred_element_type=F32)
                dk_s[:, cols] += jnp.dot(dsb, qv, preferred_element_type=F32)
                dq_s[rows, cols] += lax.dot_general(dsb, kv, tn_dims, preferred_element_type=F32)

        @pl.when(i > j)
        def _():
            tile(False)

        @pl.when(i == j)
        def _():
            tile(True)

        @pl.when(i == nt - 1)
        def _():
            for hh in range(FOX_HPS):
                src = slice(hh * PACK, hh * PACK + HEAD_DIM)
                dst_cols = slice(hh * HEAD_DIM, (hh + 1) * HEAD_DIM)
                dk_ref[:, dst_cols] = dk_s[:, src].astype(dk_ref.dtype)
                dv_ref[:, dst_cols] = dv_s[:, src].astype(dv_ref.dtype)
                dc_ref[:, dst_cols] = jnp.broadcast_to(dc_s[hh], (FOX_T, HEAD_DIM))

        @pl.when(t == len(pairs) - 1)
        def _():
            for hh in range(FOX_HPS):
                dq_ref[:, hh * HEAD_DIM:(hh + 1) * HEAD_DIM] = (
                    dq_s[:, hh * PACK:hh * PACK + HEAD_DIM] * ATTN_SCALE).astype(dq_ref.dtype)
            dr_ref[...] = dr_s[...]

    pairs = [(i, j) for j in range(nt) for i in range(j, nt)]
    i_tab = jnp.asarray([p[0] for p in pairs], jnp.int32)
    j_tab = jnp.asarray([p[1] for p in pairs], jnp.int32)
    wide, narrow = FOX_HPS * PACK, FOX_HPS * HEAD_DIM
    qs = pl.BlockSpec((FOX_T, wide), lambda hp, t, it, jt: (it[t], hp))
    ks = pl.BlockSpec((FOX_T, wide), lambda hp, t, it, jt: (jt[t], hp))
    whole = pl.BlockSpec((S, narrow), lambda hp, t, it, jt: (0, hp))
    cs = pl.BlockSpec((FOX_T, narrow), lambda hp, t, it, jt: (jt[t], hp))
    rs = pl.BlockSpec((FOX_HPS, 1, S), lambda hp, t, it, jt: (hp, 0, 0))
    shp = jax.ShapeDtypeStruct((S, FOX_WIDTH), _CD)
    grid_spec = pltpu.PrefetchScalarGridSpec(
        num_scalar_prefetch=2, grid=(N_FOX_HEADS // FOX_HPS, len(pairs)), in_specs=[qs, ks, ks, qs],
        out_specs=[whole, cs, cs, cs, rs],
        scratch_shapes=[pltpu.VMEM((S, wide), F32), pltpu.VMEM((FOX_T, wide), F32),
                        pltpu.VMEM((FOX_T, wide), F32), pltpu.VMEM((FOX_HPS, FOX_T, 1), F32),
                        pltpu.VMEM((FOX_HPS, 1, S), F32)])
    return _pcall(body, name=name, grid_spec=grid_spec,
                  out_shape=[shp, shp, shp, jax.ShapeDtypeStruct((S, FOX_WIDTH), F32),
                             jax.ShapeDtypeStruct((N_FOX_HEADS, 1, S), F32)],
                  compiler_params=_params("parallel", "arbitrary"))(i_tab, j_tab, qp, kp, vp, dop)


def _layer_step(x, tgt, w, p, late_weights=None, grad_sink=None, after=None, first_weights=None):
    S = x.shape[0]
    after_norm, after_proj = after if after is not None else (None, None)
    h = _rms_fwd(x, p["norm_mix_g"], name="rms_mix", after=after_norm)
    if first_weights is not None:
        w = {**w, **first_weights(h)}
    qkv = _mm(h, w["qkv"][:, 3 * DIL_WIDTH:], name="proj_fox", out_dtype=_CD, tn=768, tm=2048, after=after_proj)
    dil_qkv = _proj_dil(h, w["qkv"], name="proj_dil")
    zf = _mm(h, w["f"], name="proj_f")
    gl = _mm(h, w["g"], name="proj_gate", tn=1024, out_dtype=_CD)

    dil_o, dil_l = [], []
    for g in range(N_DIL_GROUPS):
        og, lg = _dil_fwd(dil_qkv[g], g, name=f"dil_fwd{g}")
        dil_o.append(og), dil_l.append(lg)
    o_a = _dil_mix_fwd(dil_o, dil_l, name="dil_mix")

    c = _fox_cumsum(zf, p["b_fgt"], name="fox_cumsum")
    fqp, fkp, fvp = _fox_pack_fwd(qkv, c, name="fox_pack")
    o_b, flse = _fox_fwd(fqp, fkp, fvp, name="fox_fwd")

    if late_weights is not None:
        w = {**w, **late_weights(o_b)}
    y_a = _mm(o_a, w["dil_out"], name="y_a", tn=1024, out_dtype=_CD)
    y_b = _mm(o_b, w["fox_out"], name="y_b", tn=1024, out_dtype=_CD)
    merged = _gate_fwd(gl, p["b_gate"], y_a, y_b, name="gate_fwd")
    x1 = _mm(merged, w["out"], name="mix_out", add=x)

    h2 = _rms_fwd(x1, p["norm_ffn_g"], name="rms_ffn")
    gate, up, act = _ffn_in_act(h2, w["ffn_in"], name="ffn_in")
    x2 = _mm(act, w["ffn_down"], name="ffn_down", add=x1, tk=2816)

    loss, dx2, dg_final = _loss_head(x2, p["norm_final_g"], tgt, name="loss_head")

    gw_ffn_down = _mm(act, dx2, name="gw_ffn_down", ta=True, out_dtype=_CD, tm=1408)
    dgu = _d_swiglu(dx2, w["ffn_down"], gate, up, name="d_swiglu")
    dh2 = _mm(dgu, w["ffn_in"], name="d_h2", tb=True, tk=1408, b_blocks=True, tm=2048, a_halves=True)
    gw_ffn_in = _mm(h2, dgu, name="gw_ffn_in", ta=True, out_dtype=_CD, tn=1408, out_blocks=1408, b_halves=True)
    sink = grad_sink if grad_sink is not None else (lambda group, grads: None)
    tok = sink("ffn", dict(ffn_in=gw_ffn_in, ffn_down=gw_ffn_down))
    dx1, dg_ffn = _rms_bwd(x1, p["norm_ffn_g"], dh2, dx2, name="rms_ffn_bwd", after=tok)

    dmerged = _mm(dx1, w["out"], name="d_merged", tb=True, out_dtype=_CD)
    gw_out = _mm(merged, dx1, name="gw_out", ta=True, out_dtype=_CD)
    dy_a, dy_b, dgl, db_gate = _gate_bwd(dmerged, gl, p["b_gate"], y_a, y_b, name="gate_bwd")
    do_a = _mm(dy_a, w["dil_out"], name="d_o_a", tb=True)
    gw_dil_out = _mm(o_a, dy_a, name="gw_dil_out", ta=True, out_dtype=_CD, tn=1024)
    do_b = _mm(dy_b, w["fox_out"], name="d_o_b", tb=True)
    gw_fox_out = _mm(o_b, dy_b, name="gw_fox_out", ta=True, out_dtype=_CD, tn=1024)
    tok = sink("mix", dict(dil_out=gw_dil_out, fox_out=gw_fox_out, out=gw_out))

    bqp, bdop = _fox_pack_bwd(qkv, c, o_b, flse, do_b, name="fox_pack_bwd", after=tok)
    dqp, dkp, dvp, dck, dcq = _fox_bwd(bqp, fkp, fvp, bdop, name="fox_bwd")
    dc = dcq[:, 0, :].T - dck.reshape(S, N_FOX_HEADS, HEAD_DIM)[:, :, 0]
    dc = jnp.pad(dc, ((0, 0), (0, F_PAD - N_FOX_HEADS)))
    dzf, db_fgt = _fox_cumsum_bwd(dc, zf, p["b_fgt"], name="fox_cumsum_bwd")

    douts = _dil_mix_bwd(do_a, dil_o, dil_l, name="dil_mix_bwd", after=tok)
    dqs, dks, dvs = [], [], []
    for g in range(N_DIL_GROUPS):
        dq, dk, dv = _dil_bwd(dil_qkv[g], douts[3 + g], douts[g], g, name=f"dil_bwd{g}")
        for parts, t in ((dqs, dq), (dks, dk), (dvs, dv)):
            parts.extend([t[0].astype(_CD), t[1].astype(_CD)])
    dqkv = jnp.concatenate(dqs + dks + dvs + [dqp, dkp, dvp], axis=1)

    gw_qkv = _mm(h, dqkv, name="gw_qkv", ta=True, out_dtype=_CD, tn=768)
    gw_g = _mm(h, dgl, name="gw_gate", ta=True, out_dtype=_CD)
    gw_f = _mm(h, dzf, name="gw_f", ta=True, out_dtype=_CD)
    tok = sink("in", dict(qkv=gw_qkv, f=gw_f, g=gw_g))
    dh = _mm(dqkv, w["qkv"], name="d_h_qkv", tb=True, tk=1920, tm=2048, after=tok)
    dh = _mm(dgl, w["g"], name="d_h_gate", tb=True, add=dh)
    dh = _mm(dzf, w["f"], name="d_h_f", tb=True, add=dh)
    dx, dg_mix = _rms_bwd(x, p["norm_mix_g"], dh, dx1, name="rms_mix_bwd")

    gw = dict(qkv=gw_qkv, f=gw_f, g=gw_g, dil_out=gw_dil_out, fox_out=gw_fox_out, out=gw_out, ffn_in=gw_ffn_in,
              ffn_down=gw_ffn_down)
    small = dict(norm_mix_g=dg_mix, b_fgt=db_fgt, b_gate=db_gate, norm_ffn_g=dg_ffn, norm_final_g=dg_final)
    return loss, dx, gw, small


def _position():
    return lax.axis_index("x"), lax.axis_index("y"), lax.axis_index("c")


def _other_chips(x, y):
    return [(1 - x, y), (x, 1 - y), (1 - x, 1 - y)]


ROW_TILE = 16


def _row_chunks(rows, want=4):
    n = want
    while n > 1 and rows % (n * ROW_TILE):
        n //= 2
    return n


SEM_SPEC = pl.BlockSpec(memory_space=pltpu.SEMAPHORE)
ANY_SPEC = pl.BlockSpec(memory_space=pl.ANY)
DATAFLOW = pltpu.SideEffectType.DATAFLOW_SIDE_EFFECTING


def _in_hbm(a):
    return pltpu.with_memory_space_constraint(a, pltpu.HBM)


def _split_copy_start(srcs, land_shapes, copies, after, *, name):
    n, m = len(srcs), len(land_shapes)

    def body(*refs):
        src_refs, land_refs = refs[:n], refs[n:n + m]
        send_sems, recv_sems = refs[n + m + 1], refs[n + m + 2]
        token = refs[-1]
        x, y, c = _position()
        for k, (src, dst, peer) in enumerate(copies(x, y, c, src_refs, land_refs)):
            pltpu.make_async_remote_copy(src_ref=src, dst_ref=dst, send_sem=send_sems.at[k], recv_sem=recv_sems.at[k],
                                         device_id=peer, device_id_type=MESH).start()
        token[...] = jnp.zeros_like(token)

    lands = [lax.empty(s.shape, s.dtype) for s in land_shapes]
    count = len(copies(0, 0, 0, srcs, lands))
    out = _pcall(
        body, name=name,
        out_shape=(pltpu.SemaphoreType.DMA((count,)), pltpu.SemaphoreType.DMA((count,)),
                   *[pltpu.HBM(s.shape, s.dtype) for s in srcs], *[pltpu.HBM(s.shape, s.dtype) for s in land_shapes],
                   jax.ShapeDtypeStruct((8, 128), F32)),
        in_specs=[HBM_SPEC] * (n + m) + [ANY_SPEC],
        out_specs=(SEM_SPEC, SEM_SPEC, *[HBM_SPEC] * (n + m), pl.BlockSpec(memory_space=pltpu.VMEM)),
        input_output_aliases={k: 2 + k for k in range(n + m)},
        compiler_params=pltpu.CompilerParams(has_side_effects=DATAFLOW),
    )(*[_in_hbm(s) for s in srcs], *[_in_hbm(l) for l in lands], after)
    return out[0], out[1], list(out[2:2 + n]), list(out[2 + n:2 + n + m]), out[-1]


def _split_copy_wait(send_sems, recv_sems, srcs, lands, copies, after, *, name):
    n, m = len(srcs), len(lands)

    def body(*refs):
        src_refs, land_refs = refs[:n], refs[n:n + m]
        send, recv = refs[n + m], refs[n + m + 1]
        x, y, c = _position()
        for k, (src, dst, peer) in enumerate(copies(x, y, c, src_refs, land_refs)):
            cp = pltpu.make_async_remote_copy(src_ref=src, dst_ref=dst, send_sem=send.at[k], recv_sem=recv.at[k],
                                              device_id=peer, device_id_type=MESH)
            cp.wait_send()
            cp.wait_recv()

    afters = list(after) if isinstance(after, (list, tuple)) else [after]
    out = _pcall(
        body, name=name,
        out_shape=tuple(pltpu.HBM(s.shape, s.dtype) for s in list(srcs) + list(lands)),
        in_specs=[HBM_SPEC] * (n + m) + [SEM_SPEC, SEM_SPEC] + [ANY_SPEC] * len(afters),
        out_specs=tuple([HBM_SPEC] * (n + m)),
        input_output_aliases={k: k for k in range(n + m)},
        compiler_params=pltpu.CompilerParams(has_side_effects=DATAFLOW),
    )(*srcs, *lands, send_sems, recv_sems, *afters)
    return list(out[:n]), list(out[n:])


def _gather_copies(x, y, c, shard_refs, land_refs):
    out = []
    for s, l in zip(shard_refs, land_refs):
        half = s.shape[0] // 2
        nq = _row_chunks(half)
        for cx, cy in _other_chips(x, y):
            for q in range(nq):
                rows = pl.ds(c * half + q * (half // nq), half // nq)
                out.append((s.at[rows, :], l.at[2 * x + y, rows, :], (cx, cy, c)))
    return out


def _gather_whole_copies(x, y, c, shard_refs, land_refs):
    out = []
    for s, l in zip(shard_refs, land_refs):
        nq = _row_chunks(s.shape[0])
        for cx, cy in _other_chips(x, y):
            for q in range(nq):
                rows = pl.ds(q * (s.shape[0] // nq), s.shape[0] // nq)
                out.append((s.at[rows, :], l.at[2 * x + y, rows, :], (cx, cy, c)))
    return out


def _scatter_all_copies(x, y, c, block_refs, land_refs):
    out = []
    for g, l in zip(block_refs, land_refs):
        half = g.shape[1] // 2
        nq = _row_chunks(half)
        size = half // nq
        for q in range(nq):
            rows = pl.ds((1 - c) * half + q * size, size)
            out.append((g.at[2 * x + y, rows, :], l.at[0, pl.ds(q * size, size), :], (x, y, 1 - c)))
        for r, (cx, cy) in enumerate(_other_chips(x, y)):
            for j in range(2):
                h = c if j == 0 else 1 - c
                for q in range(nq):
                    rows = pl.ds(h * half + q * size, size)
                    out.append((g.at[2 * cx + cy, rows, :], l.at[1 + 2 * r + j, pl.ds(q * size, size), :], (cx, cy, h)))
    return out


def _forward_halves(lands, *, name):
    n = len(lands)

    def body(*refs):
        ins = refs[:n]
        send_sems, recv_sems = refs[2 * n:]
        x, y, c = _position()
        copies = []
        for w in range(n):
            half = ins[w].shape[1] // 2
            for r, (cx, cy) in enumerate(_other_chips(x, y)):
                blk = ins[w].at[2 * cx + cy, pl.ds(c * half, half), :]
                cp = pltpu.make_async_remote_copy(src_ref=blk, dst_ref=blk, send_sem=send_sems.at[w, r],
                                                  recv_sem=recv_sems.at[w, r], device_id=(x, y, 1 - c),
                                                  device_id_type=MESH)
                cp.start()
                copies.append(cp)
        for w in range(n):
            half = ins[w].shape[1] // 2
            for r, (cx, cy) in enumerate(_other_chips(x, y)):
                blk = ins[w].at[2 * cx + cy, pl.ds((1 - c) * half, half), :]
                pltpu.make_async_remote_copy(src_ref=blk, dst_ref=blk, send_sem=send_sems.at[w, r],
                                             recv_sem=recv_sems.at[w, r], device_id=(x, y, 1 - c),
                                             device_id_type=MESH).wait_recv()
        for cp in copies:
            cp.wait_send()

    return _pcall(
        body, name=name, in_specs=[HBM_SPEC] * n, out_specs=[HBM_SPEC] * n,
        out_shape=[jax.ShapeDtypeStruct(l.shape, l.dtype) for l in lands],
        input_output_aliases={k: k for k in range(n)},
        scratch_shapes=[pltpu.SemaphoreType.DMA((n, 3)), pltpu.SemaphoreType.DMA((n, 3))],
    )(*lands)


def _share_halves(halves):
    n = len(halves)

    def body(*refs):
        ins, outs = refs[:n], refs[n:2 * n]
        send_sems, recv_sems = refs[2 * n:]
        x, y, c = _position()
        copies = []
        for w in range(n):
            cp = pltpu.make_async_remote_copy(src_ref=ins[w], dst_ref=outs[w], send_sem=send_sems.at[w],
                                              recv_sem=recv_sems.at[w], device_id=(x, y, 1 - c), device_id_type=MESH)
            cp.start()
            copies.append(cp)
        for cp in copies:
            cp.wait()

    return _pcall(
        body, name="share_halves", in_specs=[HBM_SPEC] * n, out_specs=[HBM_SPEC] * n,
        out_shape=[jax.ShapeDtypeStruct(h.shape, h.dtype) for h in halves],
        scratch_shapes=[pltpu.SemaphoreType.DMA((n,)), pltpu.SemaphoreType.DMA((n,))],
    )(*halves)


def _sum_small(part):
    rows, width = part.shape

    def body(x_ref, out_ref, all_ref, send_sems, recv_sems):
        x, y, c = _position()
        me, sibling = (x, y, c), (x, y, 1 - c)
        chips = _other_chips(x, y)

        def block(px, py, pc):
            return all_ref.at[pl.ds((4 * px + 2 * py + pc) * rows, rows), :]

        def copy(k, blk, to, src=None):
            return pltpu.make_async_remote_copy(
                src_ref=block(*blk) if src is None else src, dst_ref=block(*blk), send_sem=send_sems.at[k],
                recv_sem=recv_sems.at[k], device_id=to, device_id_type=MESH)

        all_ref[pl.ds((4 * x + 2 * y + c) * rows, rows), :] = x_ref[...]
        first = [copy(0, me, sibling, src=x_ref)]
        first += [copy(1 + j, me, (*chip, c), src=x_ref) for j, chip in enumerate(chips)]
        for cp in first:
            cp.start()
        passed = [copy(4 + j, (*chip, c), sibling) for j, chip in enumerate(chips)]
        for j, chip in enumerate(chips):
            copy(1 + j, (*chip, c), me).wait_recv()
            passed[j].start()
        copy(0, sibling, me).wait_recv()
        for j, chip in enumerate(chips):
            copy(4 + j, (*chip, 1 - c), me).wait_recv()
        for cp in first + passed:
            cp.wait_send()
        total = all_ref[0:rows, :]
        for d in range(1, 8):
            total = total + all_ref[d * rows:(d + 1) * rows, :]
        out_ref[...] = total

    vm = pl.BlockSpec(memory_space=pltpu.VMEM)
    return _pcall(
        body, name="sum_small", in_specs=[vm], out_specs=vm, out_shape=jax.ShapeDtypeStruct((rows, width), F32),
        scratch_shapes=[pltpu.VMEM((8 * rows, width), F32), pltpu.SemaphoreType.DMA((7,)), pltpu.SemaphoreType.DMA((7,))],
    )(part)


def _row_tile(R, C, itemsize=4, budget=1 << 20):
    for t in (512, 256, 128, 64, 32, 16, 8):
        if R % t == 0 and t * C * itemsize <= budget:
            return t
    return R


def _add_all(g, recv, where, *, name):
    _, R, C = g.shape
    half = R // 2
    t = _row_tile(half, C)
    nb = half // t

    def body(w_ref, g_ref, r_ref, o_ref):
        total = g_ref[0].astype(F32)
        for k in range(7):
            total = total + r_ref[k].astype(F32)
        o_ref[...] = total

    grid_spec = pltpu.PrefetchScalarGridSpec(
        num_scalar_prefetch=1, grid=(nb,),
        in_specs=[pl.BlockSpec((1, t, C), lambda i, wr: (wr[0], wr[1] * nb + i, 0)),
                  pl.BlockSpec((7, t, C), lambda i, wr: (0, i, 0))],
        out_specs=pl.BlockSpec((t, C), lambda i, wr: (i, 0)))
    return _pcall(body, name=name, grid_spec=grid_spec, out_shape=jax.ShapeDtypeStruct((half, C), F32),
                  compiler_params=_params("parallel"))(where, g, recv)


def _adamw(w, g, m, v, *, name):
    R, C = w.shape
    t = _row_tile(R, C)
    c1 = 1.0 - ADAM_B1 ** ADAM_STEP
    c2 = 1.0 - ADAM_B2 ** ADAM_STEP

    def body(w_ref, g_ref, m_ref, v_ref, d_ref, nm_ref, nv_ref):
        gv = g_ref[...]
        mn = ADAM_B1 * m_ref[...] + (1.0 - ADAM_B1) * gv
        vn = ADAM_B2 * v_ref[...] + (1.0 - ADAM_B2) * (gv * gv)
        d_ref[...] = -ADAM_LR * ((mn / c1) / (jnp.sqrt(vn / c2) + ADAM_EPS) + ADAM_WD * w_ref[...])
        nm_ref[...] = mn
        nv_ref[...] = vn

    blk = pl.BlockSpec((t, C), lambda i: (i, 0))
    shp = jax.ShapeDtypeStruct((R, C), F32)
    return _pcall(body, name=name, grid=(R // t,), in_specs=[blk] * 4, out_specs=[blk] * 3, out_shape=[shp] * 3,
                  compiler_params=_params("parallel"))(w, g, m, v)


def _adamw_halves(w, mine, theirs, m, v, core, *, name):
    R, C = w.shape
    half = R // 2
    t = _row_tile(half, C)
    nbh = half // t
    c1 = 1.0 - ADAM_B1 ** ADAM_STEP
    c2 = 1.0 - ADAM_B2 ** ADAM_STEP

    def body(core_ref, w_ref, a_ref, b_ref, m_ref, v_ref, g_ref, d_ref, nm_ref, nv_ref):
        gv = jnp.where(pl.program_id(0) // nbh == core_ref[0], a_ref[...], b_ref[...])
        mn = ADAM_B1 * m_ref[...] + (1.0 - ADAM_B1) * gv
        vn = ADAM_B2 * v_ref[...] + (1.0 - ADAM_B2) * (gv * gv)
        g_ref[...] = gv
        d_ref[...] = -ADAM_LR * ((mn / c1) / (jnp.sqrt(vn / c2) + ADAM_EPS) + ADAM_WD * w_ref[...])
        nm_ref[...] = mn
        nv_ref[...] = vn

    blk = pl.BlockSpec((t, C), lambda i, cr: (i, 0))
    hblk = pl.BlockSpec((t, C), lambda i, cr: (i % nbh, 0))
    shp = jax.ShapeDtypeStruct((R, C), F32)
    grid_spec = pltpu.PrefetchScalarGridSpec(num_scalar_prefetch=1, grid=(2 * nbh,),
                                             in_specs=[blk, hblk, hblk, blk, blk], out_specs=[blk] * 4)
    return _pcall(body, name=name, grid_spec=grid_spec, out_shape=[shp] * 4,
                  compiler_params=_params("parallel"))(core, w, mine, theirs, m, v)


BIG = ("w_in", "w_dil_out", "w_fox_out", "w_out", "w_ffn_in", "w_ffn_down")
SMALL = ("norm_mix_g", "b_fgt", "b_gate", "norm_ffn_g", "norm_final_g")
ORDER = ("norm_mix_g", "w_in", "b_fgt", "b_gate", "w_dil_out", "w_fox_out", "w_out", "norm_ffn_g", "w_ffn_in",
         "w_ffn_down", "norm_final_g")
SMALL_ROWS = {"norm_mix_g": (0, 1), "b_gate": (1, 3), "norm_ffn_g": (3, 4), "norm_final_g": (4, 5), "b_fgt": (5, 6)}


def _columns_to_blocks(full, ncol):
    K = full.shape[0]
    return full.reshape(K, 4, ncol).transpose(1, 0, 2)


def _blocks_to_columns(blocks):
    n, K, ncol = blocks.shape
    return blocks.transpose(1, 0, 2).reshape(K, n * ncol)


def kernel(x, norm_mix_g, w_in, b_fgt, b_gate, w_dil_out, w_fox_out, w_out, norm_ffn_g, w_ffn_in, w_ffn_down, norm_final_g, loss_target, m_norm_mix_g, m_w_in, m_b_fgt, m_b_gate, m_w_dil_out, m_w_fox_out, m_w_out, m_norm_ffn_g, m_w_ffn_in, m_w_ffn_down, m_norm_final_g, v_norm_mix_g, v_w_in, v_b_fgt, v_b_gate, v_w_dil_out, v_w_fox_out, v_w_out, v_norm_ffn_g, v_w_ffn_in, v_w_ffn_down, v_norm_final_g):
    weights = dict(norm_mix_g=norm_mix_g, w_in=w_in, b_fgt=b_fgt, b_gate=b_gate, w_dil_out=w_dil_out,
                   w_fox_out=w_fox_out, w_out=w_out, norm_ffn_g=norm_ffn_g, w_ffn_in=w_ffn_in, w_ffn_down=w_ffn_down,
                   norm_final_g=norm_final_g)
    m_in = dict(norm_mix_g=m_norm_mix_g, w_in=m_w_in, b_fgt=m_b_fgt, b_gate=m_b_gate, w_dil_out=m_w_dil_out,
                w_fox_out=m_w_fox_out, w_out=m_w_out, norm_ffn_g=m_norm_ffn_g, w_ffn_in=m_w_ffn_in,
                w_ffn_down=m_w_ffn_down, norm_final_g=m_norm_final_g)
    v_in = dict(norm_mix_g=v_norm_mix_g, w_in=v_w_in, b_fgt=v_b_fgt, b_gate=v_b_gate, w_dil_out=v_w_dil_out,
                w_fox_out=v_w_fox_out, w_out=v_w_out, norm_ffn_g=v_norm_ffn_g, w_ffn_in=v_w_ffn_in,
                w_ffn_down=v_w_ffn_down, norm_final_g=v_norm_final_g)
    c = lax.axis_index("c")
    chip = 2 * lax.axis_index("x") + lax.axis_index("y")

    shards = {n: weights[n][0].astype(_CD) for n in BIG}
    in_shape = jax.ShapeDtypeStruct((4,) + shards["w_in"].shape, _CD)
    send_i, recv_i, in_src, in_land, token_in = _split_copy_start(
        [shards["w_in"]], [in_shape], _gather_copies, norm_mix_g, name="gather_in_start")
    late = BIG[1:]
    send_g, recv_g, late_src, late_land, token = _split_copy_start(
        [shards[n] for n in late], [jax.ShapeDtypeStruct((4,) + shards[n].shape, _CD) for n in late],
        _gather_whole_copies, token_in, name="gather_late_start")
    adam_in = [t[0] + token_in[0, 0] for t in (w_in, m_w_in, v_w_in)]
    p = dict(norm_mix_g=norm_mix_g, b_fgt=jnp.pad(b_fgt, ((0, 0), (0, F_PAD - N_FOX_HEADS))), b_gate=b_gate,
             norm_ffn_g=norm_ffn_g, norm_final_g=norm_final_g.reshape(1, D_MODEL))

    def first_weights(after):
        own, lands = _split_copy_wait(send_i, recv_i, in_src, in_land, _gather_copies, [after] + adam_in,
                                      name="gather_in_wait")
        (g_in,) = _forward_halves(lands, name="gather_in_forward")
        full_in = _blocks_to_columns(lax.dynamic_update_index_in_dim(g_in, own[0], chip, 0))
        o3 = QKV_COLS
        o4 = o3 + N_FOX_HEADS
        return dict(qkv=full_in[:, :o3], f=jnp.pad(full_in[:, o3:o4], ((0, 0), (0, F_PAD - N_FOX_HEADS))),
                    g=full_in[:, o4:])

    def late_weights(after):
        own, lands = _split_copy_wait(send_g, recv_g, late_src, late_land, _gather_whole_copies, after,
                                      name="gather_late_wait")
        g_dil, g_fox, g_out, g_ffn_in, g_ffn_down = [
            lax.dynamic_update_index_in_dim(l, s, chip, 0) for l, s in zip(lands, own)]
        return dict(dil_out=_blocks_to_columns(g_dil), fox_out=_blocks_to_columns(g_fox),
                    out=g_out.reshape(D_MODEL, D_MODEL), ffn_in=g_ffn_in,
                    ffn_down=g_ffn_down.reshape(D_FF, D_MODEL))

    def to_blocks(n, full):
        shape = weights[n].shape
        if full.ndim == 3:
            return full
        if n in ("w_out", "w_ffn_down"):
            return full.reshape(4, shape[1], shape[2])
        return _columns_to_blocks(full, shape[2])

    in_flight = {}

    def grad_sink(group, gw):
        if group == "in":
            named = {"w_in": jnp.concatenate([gw["qkv"], gw["f"][:, :N_FOX_HEADS], gw["g"]], axis=1)}
        else:
            named = {"w_" + k: v for k, v in gw.items()}
        srcs = [to_blocks(n, named[n]) for n in named]
        lands = [jax.ShapeDtypeStruct((7, s.shape[1] // 2, s.shape[2]), s.dtype) for s in srcs]
        started = _split_copy_start(srcs, lands, _scatter_all_copies, next(iter(gw.values())),
                                    name=f"scatter_{group}_start")
        in_flight[group] = (list(named), started)
        return started[-1]

    loss_part, grad_x, gw, small = _layer_step(x[0], loss_target[0], {}, p, late_weights, grad_sink,
                                               (token_in, token), first_weights)

    halves = {}
    where = jnp.stack([chip, c]).astype(jnp.int32)
    for group, (names, (send_s, recv_s, srcs, lands, _)) in in_flight.items():
        srcs, recv = _split_copy_wait(send_s, recv_s, srcs, lands, _scatter_all_copies, grad_x,
                                      name=f"scatter_{group}_wait")
        halves.update({n: _add_all(s, r, where, name=f"add_all_{n}") for n, s, r in zip(names, srcs, recv)})
    halves = [halves[n] for n in BIG]
    grad_halves = dict(zip(BIG, zip(halves, _share_halves(halves))))
    grads = {}

    packed = jnp.concatenate([
        small["norm_mix_g"], small["b_gate"].reshape(2, D_MODEL), small["norm_ffn_g"], small["norm_final_g"],
        jnp.pad(small["b_fgt"], ((0, 0), (0, D_MODEL - F_PAD))), jnp.pad(loss_part, ((0, 0), (0, D_MODEL - 1))),
        jnp.zeros((1, D_MODEL), F32)], axis=0)
    summed = _sum_small(packed)
    for n in SMALL:
        lo, hi = SMALL_ROWS[n]
        grads[n] = summed[lo:hi].reshape(1, -1)[:, :weights[n].size]
    loss = summed[6, 0]

    out_g, out_d, out_m, out_v = {}, {}, {}, {}
    core = jnp.reshape(c, (1,)).astype(jnp.int32)
    for n in ORDER:
        shape = weights[n].shape
        two_d = shape[1:] if len(shape) == 3 else (1, weights[n].size)
        wmv = adam_in if n == "w_in" else [t.reshape(two_d) for t in (weights[n], m_in[n], v_in[n])]
        if n in grad_halves:
            mine, theirs = grad_halves[n]
            g2, d2, m2, v2 = _adamw_halves(wmv[0], mine, theirs, wmv[1], wmv[2], core, name=f"adamw_{n}")
        else:
            g2 = grads[n].reshape(two_d)
            d2, m2, v2 = _adamw(wmv[0], g2, wmv[1], wmv[2], name=f"adamw_{n}")
        out_g[n], out_d[n], out_m[n], out_v[n] = (g2.reshape(shape), d2.reshape(shape), m2.reshape(shape),
                                                  v2.reshape(shape))
    return (loss, grad_x[None], *[out_g[n] for n in ORDER], *[out_d[n] for n in ORDER],
            *[out_m[n] for n in ORDER], *[out_v[n] for n in ORDER])
```

```python
import numpy as np
import jax
import jax.numpy as jnp
from jax import lax
from jax.experimental import pallas as pl
from jax.experimental.pallas import tpu as pltpu

F32 = jnp.float32
_CD = jnp.bfloat16

D_MODEL = 1024
HEAD_DIM = 64
DIL_PAIRS = ((128, 1), (512, 4), (2048, 16))
N_DIL_GROUPS = 3
DIL_HEADS = 4
DIL_W = 128
DIL_OUT = DIL_HEADS * HEAD_DIM
DIL_WIDTH = N_DIL_GROUPS * DIL_OUT
N_FOX_HEADS = 8
FOX_WIDTH = N_FOX_HEADS * HEAD_DIM
D_FF = 2816
QKV_COLS = 3 * DIL_WIDTH + 3 * FOX_WIDTH
F_PAD = 128
RMS_EPS = 1e-6
NEG_INF = -1e30
ATTN_SCALE = HEAD_DIM ** -0.5
ADAM_LR, ADAM_B1, ADAM_B2, ADAM_EPS, ADAM_WD, ADAM_STEP = 0.001, 0.9, 0.999, 1e-08, 0.01, 10

VMEM_LIMIT = 48 * 1024 * 1024
VMEM_LIMIT_RESIDENT = 56 * 1024 * 1024
LANES = 128
MESH = pl.DeviceIdType.MESH
HBM_SPEC = pl.BlockSpec(memory_space=pltpu.HBM)


def _pcall(body, after=None, **kw):
    if after is None:
        return pl.pallas_call(body, **kw)
    n_in = len(kw["in_specs"])
    kw["in_specs"] = list(kw["in_specs"]) + [pl.BlockSpec(memory_space=pl.ANY)]

    def tied(*refs):
        return body(*refs[:n_in], *refs[n_in + 1:])

    call = pl.pallas_call(tied, **kw)
    return lambda *args: call(*args, after)


def _params(*sem):
    return pltpu.CompilerParams(dimension_semantics=sem, vmem_limit_bytes=VMEM_LIMIT)


def _pick(dim, pref):
    t = (min(pref, dim) // 128) * 128
    while t >= 128:
        if dim % t == 0:
            return t
        t -= 128
    return dim


def _mm(a, b, *, name, ta=False, tb=False, out_dtype=F32, add=None, tm=1024, tn=512, tk=2048, after=None,
        b_blocks=False, out_blocks=None, a_halves=False, b_halves=False, rms_bwd=None):
    if a_halves:
        M, K = a.shape[1], 2 * a.shape[2]
    elif ta:
        K, M = a.shape
    else:
        M, K = a.shape
    if b_halves:
        b_rows, b_cols = b.shape[1], 2 * b.shape[2]
    else:
        b_rows, b_cols = (b.shape[1], b.shape[0] * b.shape[2]) if b_blocks else b.shape
    if tb:
        N, K2 = b_rows, b_cols
    else:
        K2, N = b_rows, b_cols
    assert K == K2, (a.shape, b.shape)
    shard = b.shape[2] if b_blocks else None
    tm = _pick(M, tm)
    tn = _pick(shard if (b_blocks and not tb) else (out_blocks or N), tn)
    tk = _pick(shard if (b_blocks and tb) else K, tk)
    nk = K // tk
    dn = (((0 if ta else 1,), (1 if tb else 0,)), ((), ()))
    has_add = add is not None
    assert not (has_add and out_blocks)
    has_norm = rms_bwd is not None
    if has_norm:
        tn = N
        assert not out_blocks and out_dtype == F32

    def body(*refs):
        a_ref, b_ref = refs[0], refs[1]
        rest = list(refs[2:])
        add_ref = rest.pop(0) if has_add else None
        x_ref, g_ref, dres_ref = (rest.pop(0), rest.pop(0), rest.pop(0)) if has_norm else (None, None, None)
        o_ref = rest.pop(0)
        dg_ref = rest.pop(0) if has_norm else None
        bv = b_ref[0] if b_blocks else b_ref[...]
        p = lax.dot_general(a_ref[...].astype(_CD), bv.astype(_CD), dn, preferred_element_type=F32)

        def finish(r):
            if has_add:
                r = r + add_ref[...]
            if has_norm:
                xv = x_ref[...]
                rs = lax.rsqrt(jnp.mean(xv * xv, axis=-1, keepdims=True) + RMS_EPS)
                xh = xv * rs
                dxh = r * g_ref[...]
                o_ref[...] = dres_ref[...] + rs * (dxh - xh * jnp.mean(dxh * xh, axis=-1, keepdims=True))
                part = jnp.sum(r * xh, axis=0, keepdims=True)
                first = pl.program_id(0) == 0

                @pl.when(first)
                def _():
                    dg_ref[...] = part

                @pl.when(jnp.logical_not(first))
                def _():
                    dg_ref[...] += part
            elif out_blocks:
                o_ref[0] = r.astype(out_dtype)
            else:
                o_ref[...] = r.astype(out_dtype)

        if nk == 1:
            finish(p)
        else:
            acc_ref = rest.pop(0)
            k = pl.program_id(2)

            @pl.when(k == 0)
            def _():
                acc_ref[...] = p

            @pl.when(k > 0)
            def _():
                acc_ref[...] += p

            @pl.when(k == nk - 1)
            def _():
                finish(acc_ref[...])

    if a_halves:
        ka = (K // 2) // tk
        a_spec = pl.BlockSpec((None, tm, tk), lambda i, j, k: (k // ka, i, k % ka))
    else:
        a_spec = pl.BlockSpec((tk, tm), lambda i, j, k: (k, i)) if ta else pl.BlockSpec((tm, tk), lambda i, j, k: (i, k))
    if b_halves:
        nb_ = (N // 2) // tn
        b_spec = pl.BlockSpec((None, tk, tn), lambda i, j, k: (j // nb_, k, j % nb_))
    elif b_blocks and tb:
        per = shard // tk
        b_spec = pl.BlockSpec((1, tn, tk), lambda i, j, k: (k // per, j, k % per))
    elif b_blocks:
        per = shard // tn
        b_spec = pl.BlockSpec((1, tk, tn), lambda i, j, k: (j // per, k, j % per))
    else:
        b_spec = pl.BlockSpec((tn, tk), lambda i, j, k: (j, k)) if tb else pl.BlockSpec((tk, tn), lambda i, j, k: (k, j))
    if out_blocks:
        oper = out_blocks // tn
        o_spec = pl.BlockSpec((1, tm, tn), lambda i, j, k: (j // oper, i, j % oper))
        out_shape = jax.ShapeDtypeStruct((N // out_blocks, M, out_blocks), out_dtype)
    else:
        o_spec = pl.BlockSpec((tm, tn), lambda i, j, k: (i, j))
        out_shape = jax.ShapeDtypeStruct((M, N), out_dtype)
    in_specs = [a_spec, b_spec] + ([o_spec] if has_add else [])
    args = (a, b) + ((add,) if has_add else ())
    out_specs, semantics = o_spec, ("parallel", "parallel", "arbitrary")
    if has_norm:
        vec = pl.BlockSpec((1, N), lambda i, j, k: (0, 0))
        in_specs += [o_spec, vec, o_spec]
        args += tuple(rms_bwd)
        out_specs, out_shape = [o_spec, vec], [out_shape, jax.ShapeDtypeStruct((1, N), F32)]
        semantics = ("arbitrary", "arbitrary", "arbitrary")
    return _pcall(
        body, after, name=name, grid=(M // tm, N // tn, nk), in_specs=in_specs, out_specs=out_specs,
        out_shape=out_shape,
        scratch_shapes=[pltpu.VMEM((tm, tn), F32)] if nk > 1 else [],
        compiler_params=_params(*semantics),
    )(*args)


def _rms_fwd(x, g, *, name, tm=512, after=None):
    S, D = x.shape

    def body(x_ref, g_ref, h_ref):
        xv = x_ref[...]
        r = lax.rsqrt(jnp.mean(xv * xv, axis=-1, keepdims=True) + RMS_EPS)
        h_ref[...] = ((xv * r) * g_ref[...]).astype(h_ref.dtype)

    row = pl.BlockSpec((tm, D), lambda i: (i, 0))
    return _pcall(body, after, name=name, grid=(S // tm,), in_specs=[row, pl.BlockSpec((1, D), lambda i: (0, 0))],
                  out_specs=row, out_shape=jax.ShapeDtypeStruct((S, D), _CD), compiler_params=_params("parallel"))(x, g)


def _ffn_down_loss(act, w_down, x1, g, tgt, *, name, tm=512):
    S, D = x1.shape
    F = act.shape[1]

    def body(a_ref, b_ref, x_ref, g_ref, t_ref, loss_ref, dx_ref, dg_ref):
        xv = x_ref[...] + jnp.dot(a_ref[...].astype(_CD), b_ref[...].astype(_CD), preferred_element_type=F32)
        gv = g_ref[...]
        r = lax.rsqrt(jnp.mean(xv * xv, axis=-1, keepdims=True) + RMS_EPS)
        xh = xv * r
        err = xh * gv - t_ref[...]
        lpart = 0.5 * jnp.sum(jnp.mean(err * err, axis=-1, keepdims=True), axis=0, keepdims=True)
        dy = err * (1.0 / D)
        dxh = dy * gv
        dx_ref[...] = r * (dxh - xh * jnp.mean(dxh * xh, axis=-1, keepdims=True))
        gpart = jnp.sum(dy * xh, axis=0, keepdims=True)

        @pl.when(pl.program_id(0) == 0)
        def _():
            loss_ref[...] = lpart
            dg_ref[...] = gpart

        @pl.when(pl.program_id(0) > 0)
        def _():
            loss_ref[...] += lpart
            dg_ref[...] += gpart

    row = pl.BlockSpec((tm, D), lambda i: (i, 0))
    vec = pl.BlockSpec((1, D), lambda i: (0, 0))
    one = pl.BlockSpec((1, 1), lambda i: (0, 0))
    return _pcall(body, name=name, grid=(S // tm,),
                  in_specs=[pl.BlockSpec((tm, F), lambda i: (i, 0)), pl.BlockSpec((F, D), lambda i: (0, 0)), row, vec, row],
                  out_specs=[one, row, vec],
                  out_shape=[jax.ShapeDtypeStruct((1, 1), F32), jax.ShapeDtypeStruct((S, D), F32),
                             jax.ShapeDtypeStruct((1, D), F32)],
                  compiler_params=_params("arbitrary"))(act, w_down, x1, g, tgt)


def _sigmoid(z):
    return 1.0 / (1.0 + jnp.exp(-z))


def _gate_fwd(gl, bg, ya, yb, *, name, tm=512):
    S, D = ya.shape

    def body(za_ref, zb_ref, ba_ref, bb_ref, ya_ref, yb_ref, o_ref):
        ga = _sigmoid(za_ref[...].astype(F32) + ba_ref[...])
        gb = _sigmoid(zb_ref[...].astype(F32) + bb_ref[...])
        o_ref[...] = (ga * ya_ref[...].astype(F32) + gb * yb_ref[...].astype(F32)).astype(o_ref.dtype)

    lo = pl.BlockSpec((tm, D), lambda i: (i, 0))
    hi = pl.BlockSpec((tm, D), lambda i: (i, 1))
    vlo = pl.BlockSpec((1, D), lambda i: (0, 0))
    vhi = pl.BlockSpec((1, D), lambda i: (0, 1))
    return _pcall(body, name=name, grid=(S // tm,), in_specs=[lo, hi, vlo, vhi, lo, lo], out_specs=lo,
                  out_shape=jax.ShapeDtypeStruct((S, D), _CD), compiler_params=_params("parallel"))(gl, gl, bg, bg, ya, yb)


def _gate_bwd(dm, gl, bg, ya, yb, *, name, tm=256):
    S, D = ya.shape

    def body(dm_ref, za_ref, zb_ref, ba_ref, bb_ref, ya_ref, yb_ref, dya_ref, dyb_ref, dgl_ref, dbg_ref):
        dmv = dm_ref[...].astype(F32)
        ga = _sigmoid(za_ref[...].astype(F32) + ba_ref[...])
        gb = _sigmoid(zb_ref[...].astype(F32) + bb_ref[...])
        dya_ref[...] = (dmv * ga).astype(dya_ref.dtype)
        dyb_ref[...] = (dmv * gb).astype(dyb_ref.dtype)
        dza = dmv * ya_ref[...].astype(F32) * ga * (1.0 - ga)
        dzb = dmv * yb_ref[...].astype(F32) * gb * (1.0 - gb)
        dgl_ref[:, :D] = dza.astype(dgl_ref.dtype)
        dgl_ref[:, D:] = dzb.astype(dgl_ref.dtype)
        pa = jnp.sum(dza, axis=0, keepdims=True)
        pb = jnp.sum(dzb, axis=0, keepdims=True)

        @pl.when(pl.program_id(0) == 0)
        def _():
            dbg_ref[:, :D] = pa
            dbg_ref[:, D:] = pb

        @pl.when(pl.program_id(0) > 0)
        def _():
            dbg_ref[:, :D] += pa
            dbg_ref[:, D:] += pb

    lo = pl.BlockSpec((tm, D), lambda i: (i, 0))
    hi = pl.BlockSpec((tm, D), lambda i: (i, 1))
    vlo = pl.BlockSpec((1, D), lambda i: (0, 0))
    vhi = pl.BlockSpec((1, D), lambda i: (0, 1))
    wide = pl.BlockSpec((tm, 2 * D), lambda i: (i, 0))
    vwide = pl.BlockSpec((1, 2 * D), lambda i: (0, 0))
    return _pcall(body, name=name, grid=(S // tm,), in_specs=[lo, lo, hi, vlo, vhi, lo, lo],
                  out_specs=[lo, lo, wide, vwide],
                  out_shape=[jax.ShapeDtypeStruct((S, D), _CD), jax.ShapeDtypeStruct((S, D), _CD),
                             jax.ShapeDtypeStruct((S, 2 * D), _CD), jax.ShapeDtypeStruct((1, 2 * D), F32)],
                  compiler_params=_params("arbitrary"))(dm, gl, gl, bg, bg, ya, yb)


def _ffn_in_act(h2, w_blocks, *, name, tm=512):
    S, D = h2.shape
    _, _, C = w_blocks.shape

    def body(a_ref, bg_ref, bu_ref, g_ref, u_ref, o_ref):
        av = a_ref[...].astype(_CD)
        gv = jnp.dot(av, bg_ref[0].astype(_CD), preferred_element_type=F32)
        uv = jnp.dot(av, bu_ref[0].astype(_CD), preferred_element_type=F32)
        g_ref[...] = gv.astype(g_ref.dtype)
        u_ref[...] = uv.astype(u_ref.dtype)
        o_ref[...] = (gv * _sigmoid(gv) * uv).astype(o_ref.dtype)

    out = pl.BlockSpec((tm, C), lambda i, j: (i, j))
    shp = jax.ShapeDtypeStruct((S, 2 * C), _CD)
    return _pcall(body, name=name, grid=(S // tm, 2),
                  in_specs=[pl.BlockSpec((tm, D), lambda i, j: (i, 0)), pl.BlockSpec((1, D, C), lambda i, j: (j, 0, 0)),
                            pl.BlockSpec((1, D, C), lambda i, j: (2 + j, 0, 0))],
                  out_specs=[out, out, out], out_shape=[shp, shp, shp],
                  compiler_params=_params("parallel", "arbitrary"))(h2, w_blocks, w_blocks)


def _d_swiglu(dx, w_down, gate, up, *, name, tm=512, tn=1408):
    S, D = dx.shape
    F = w_down.shape[0]
    nt = (((1,), (1,)), ((), ()))

    def body(a_ref, b_ref, g_ref, u_ref, o_ref):
        dv = lax.dot_general(a_ref[...].astype(_CD), b_ref[...].astype(_CD), nt, preferred_element_type=F32)
        gv = g_ref[...].astype(F32)
        sg = _sigmoid(gv)
        o_ref[0] = (dv * u_ref[...].astype(F32) * (sg * (1.0 + gv * (1.0 - sg)))).astype(o_ref.dtype)
        o_ref[1] = (dv * (gv * sg)).astype(o_ref.dtype)

    tile = pl.BlockSpec((tm, tn), lambda i, j: (i, j))
    return _pcall(body, name=name, grid=(S // tm, F // tn),
                  in_specs=[pl.BlockSpec((tm, D), lambda i, j: (i, 0)), pl.BlockSpec((tn, D), lambda i, j: (j, 0)),
                            tile, tile],
                  out_specs=pl.BlockSpec((2, tm, tn), lambda i, j: (0, i, j)),
                  out_shape=jax.ShapeDtypeStruct((2, S, F), _CD),
                  compiler_params=_params("parallel", "arbitrary"))(dx, w_down, gate, up)


def _split3(x):
    hi = x.astype(jnp.bfloat16)
    r1 = x - hi.astype(F32)
    mid = r1.astype(jnp.bfloat16)
    lo = (r1 - mid.astype(F32)).astype(jnp.bfloat16)
    return hi, mid, lo


def _ones_dot_left(ones, x):
    return sum(jnp.dot(ones, p, preferred_element_type=F32) for p in _split3(x))


def _ones_dot_right(x, ones):
    return sum(jnp.dot(p, ones, preferred_element_type=F32) for p in _split3(x))


def _head_sum(x):
    n = x.shape[1]
    r = lax.broadcasted_iota(jnp.int32, (n, n), 0) // HEAD_DIM
    c = lax.broadcasted_iota(jnp.int32, (n, n), 1) // HEAD_DIM
    return _ones_dot_right(x, (r == c).astype(jnp.bfloat16))


def _log_sigmoid(z):
    e = jnp.exp(-jnp.abs(z))
    t = 1.0 + e
    log1p_e = jnp.where(t == 1.0, e, jnp.log(t) * (e / jnp.where(t == 1.0, 1.0, t - 1.0)))
    return jnp.minimum(z, 0.0) - log1p_e


def _fox_cumsum(zf, bf, *, name):
    S, W = zf.shape
    nb = S // 128

    def body(z_ref, b_ref, c_ref):
        tri = (lax.broadcasted_iota(jnp.int32, (128, 128), 0) >= lax.broadcasted_iota(jnp.int32, (128, 128), 1))
        tri = tri.astype(jnp.bfloat16)

        def step(i, carry):
            rows = pl.ds(pl.multiple_of(i * 128, 128), 128)
            lf = _log_sigmoid(z_ref[rows, :] + b_ref[...])
            cb = _ones_dot_left(tri, lf) + carry
            c_ref[rows, :] = cb
            return cb[127:128, :]

        lax.fori_loop(0, nb, step, jnp.zeros((1, W), F32))

    return _pcall(body, name=name, out_shape=jax.ShapeDtypeStruct((S, W), F32),
                  compiler_params=pltpu.CompilerParams(vmem_limit_bytes=VMEM_LIMIT))(zf, bf)


def _fox_cumsum_bwd(dc, zf, bf, *, name):
    S, W = zf.shape
    nb = S // 128

    def body(dc_ref, z_ref, b_ref, dz_ref, db_ref):
        tri = (lax.broadcasted_iota(jnp.int32, (128, 128), 0) <= lax.broadcasted_iota(jnp.int32, (128, 128), 1))
        tri = tri.astype(jnp.bfloat16)

        def step(k, carry):
            tail, acc = carry
            i = nb - 1 - k
            rows = pl.ds(pl.multiple_of(i * 128, 128), 128)
            dlf = _ones_dot_left(tri, dc_ref[rows, :]) + tail
            dz = dlf * _sigmoid(-(z_ref[rows, :] + b_ref[...]))
            dz_ref[rows, :] = dz
            return dlf[0:1, :], acc + jnp.sum(dz, axis=0, keepdims=True)

        _, acc = lax.fori_loop(0, nb, step, (jnp.zeros((1, W), F32), jnp.zeros((1, W), F32)))
        db_ref[...] = acc

    return _pcall(body, name=name,
                  out_shape=[jax.ShapeDtypeStruct((S, W), F32), jax.ShapeDtypeStruct((1, W), F32)],
                  compiler_params=pltpu.CompilerParams(vmem_limit_bytes=VMEM_LIMIT))(dc, zf, bf)


def _proj_dil(h, w_qkv, *, name, tm=1024):
    S, D = h.shape
    tn = DIL_WIDTH

    def body(a_ref, b_ref, *rest):
        outs, acc = rest[:N_DIL_GROUPS], rest[N_DIL_GROUPS]
        prod = jnp.dot(a_ref[...].astype(_CD), b_ref[...].astype(_CD), preferred_element_type=F32)
        for k in range(tn // LANES):
            acc[k] = prod[:, k * LANES:(k + 1) * LANES]
        for g, (_, d) in enumerate(DIL_PAIRS):
            for half in range(DIL_OUT // LANES):
                k = g * (DIL_OUT // LANES) + half
                cols = slice(half * LANES, (half + 1) * LANES)
                for r in range(d):
                    rows = pl.ds(r, tm // d, stride=d) if d > 1 else slice(None)
                    outs[g][0, r, :, cols] = acc[k, rows, :].astype(outs[g].dtype)

    out_specs = [pl.BlockSpec((1, d, tm // d, DIL_OUT), lambda i, j: (j, 0, i, 0)) for _, d in DIL_PAIRS]
    out_shape = [jax.ShapeDtypeStruct((3, d, S // d, DIL_OUT), _CD) for _, d in DIL_PAIRS]
    outs = _pcall(body, name=name, grid=(S // tm, 3),
                  in_specs=[pl.BlockSpec((tm, D), lambda i, j: (i, 0)), pl.BlockSpec((D, tn), lambda i, j: (0, j))],
                  out_specs=out_specs, out_shape=out_shape, scratch_shapes=[pltpu.VMEM((tn // LANES, tm, LANES), F32)],
                  compiler_params=_params("parallel", "arbitrary"))(h, w_qkv)
    return [o.reshape(3, S, DIL_OUT) for o in outs]


def _dil_start(block, S, dilation):
    sub = S // dilation
    u0 = block * DIL_W
    return (u0 % sub) * dilation + u0 // sub


def _dil_slopes(group):
    h = np.arange(1, N_DIL_GROUPS * DIL_HEADS + 1, dtype=np.float32)
    s = (np.float32(2.0) ** (np.float32(-8.0) * h / np.float32(N_DIL_GROUPS * DIL_HEADS))).astype(np.float32)
    return [float(v) for v in s.reshape(N_DIL_GROUPS, DIL_HEADS)[group]]


def _dil_tiles(i, n, blocks_per_seq):
    qi = lax.broadcasted_iota(jnp.int32, (DIL_W, 2 * DIL_W), 0)
    kj = lax.broadcasted_iota(jnp.int32, (DIL_W, 2 * DIL_W), 1)
    rel = qi + DIL_W - kj
    first = ((4 * n + i) % blocks_per_seq) == 0
    valid = jnp.logical_and(jnp.logical_and(rel >= 0, rel <= DIL_W), jnp.logical_or(kj >= DIL_W, jnp.logical_not(first)))
    return valid, rel.astype(F32)


def _dil_window(cur_ref, prev_ref, i, cols):
    if i > 0:
        return cur_ref[(i - 1) * DIL_W:(i + 1) * DIL_W, cols]
    return jnp.concatenate([prev_ref[:, cols], cur_ref[:DIL_W, cols]], axis=0)


CHUNK = 4 * DIL_W


def _dil_rows(block, S, dilation):
    start = _dil_start(block, S, dilation)
    return pl.ds(start, DIL_W, stride=dilation) if dilation > 1 else pl.ds(start, DIL_W)


def SPLIT(S):
    return (DIL_OUT // LANES, S, LANES)


def _dil_fwd(qkv, group, *, name):
    S = qkv.shape[1]
    dilation = DIL_PAIRS[group][1]
    bps = (S // dilation) // DIL_W
    slopes = _dil_slopes(group)
    nt = (((1,), (1,)), ((), ()))

    def body(q_ref, k_ref, v_ref, kp_ref, vp_ref, on_ref, ln_ref, o_ref, l_ref):
        n = pl.program_id(0)
        for i in range(4):
            valid, rel = _dil_tiles(i, n, bps)
            rows = slice(i * DIL_W, (i + 1) * DIL_W)
            for h in range(DIL_HEADS):
                cols = slice(h * HEAD_DIM, (h + 1) * HEAD_DIM)
                qh = q_ref[rows, cols]
                k2, v2 = _dil_window(k_ref, kp_ref, i, cols), _dil_window(v_ref, vp_ref, i, cols)
                s = lax.dot_general(qh, k2, nt, preferred_element_type=F32) * ATTN_SCALE - (slopes[h] * dilation) * rel
                s = jnp.where(valid, s, NEG_INF)
                m = jnp.max(s, axis=-1, keepdims=True)
                p = jnp.exp(s - m)
                den = jnp.sum(p, axis=-1, keepdims=True)
                acc = jnp.dot(p.astype(_CD), v2, preferred_element_type=F32)
                o_ref[rows, cols] = acc / den
                l_ref[rows, cols] = jnp.broadcast_to(m + jnp.log(den), (DIL_W, HEAD_DIM))
        for i in range(4):
            rows = slice(i * DIL_W, (i + 1) * DIL_W)
            nat = _dil_rows(4 * n + i, S, dilation)
            for half in range(DIL_OUT // LANES):
                cols = slice(half * LANES, (half + 1) * LANES)
                on_ref[half, nat, :] = o_ref[rows, cols]
                ln_ref[half, nat, :] = l_ref[rows, cols]

    def cur(which):
        return pl.BlockSpec((None, CHUNK, DIL_OUT), lambda n: (which, n, 0))

    def prev(which):
        return pl.BlockSpec((None, DIL_W, DIL_OUT), lambda n: (which, jnp.maximum(4 * n - 1, 0), 0))

    whole = pl.BlockSpec(SPLIT(S), lambda n: (0, 0, 0))
    return _pcall(body, name=name, grid=(S // CHUNK,), in_specs=[cur(0), cur(1), cur(2), prev(1), prev(2)],
                  out_specs=[whole, whole],
                  out_shape=[jax.ShapeDtypeStruct(SPLIT(S), F32), jax.ShapeDtypeStruct(SPLIT(S), F32)],
                  scratch_shapes=[pltpu.VMEM((CHUNK, DIL_OUT), F32), pltpu.VMEM((CHUNK, DIL_OUT), F32)],
                  compiler_params=_params("arbitrary"))(qkv, qkv, qkv, qkv, qkv)


STAT_OFFSET = HEAD_DIM // 2


def _dil_bwd(qkv, stats, do, group, *, name):
    S = qkv.shape[1]
    dilation = DIL_PAIRS[group][1]
    bps = (S // dilation) // DIL_W
    slopes = _dil_slopes(group)
    nchunk = S // CHUNK
    nt = (((1,), (1,)), ((), ()))
    tn = (((0,), (0,)), ((), ()))

    def body(q_ref, k_ref, v_ref, kp_ref, vp_ref, ln_ref, don_ref, dqn_ref, dkn_ref, dvn_ref,
             dk_s, dv_s, l_ref, do_ref, dq_ref):
        step = pl.program_id(0)
        n = nchunk - 1 - step
        for i in range(4):
            rows = slice(i * DIL_W, (i + 1) * DIL_W)
            nat = _dil_rows(4 * n + i, S, dilation)
            for half in range(DIL_OUT // LANES):
                cols = slice(half * LANES, (half + 1) * LANES)
                l_ref[rows, cols] = ln_ref[half, nat, :]
                do_ref[rows, cols] = don_ref[half, nat, :]

        @pl.when(step == 0)
        def _():
            dk_s[:, CHUNK:] = jnp.zeros((DIL_OUT, DIL_W), F32)
            dv_s[:, CHUNK:] = jnp.zeros((DIL_OUT, DIL_W), F32)

        dk_s[:, :CHUNK] = jnp.zeros((DIL_OUT, CHUNK), F32)
        dv_s[:, :CHUNK] = jnp.zeros((DIL_OUT, CHUNK), F32)
        for i in range(4):
            valid, rel = _dil_tiles(i, n, bps)
            rows = slice(i * DIL_W, (i + 1) * DIL_W)
            window = slice(i * DIL_W, (i + 2) * DIL_W)
            for h in range(DIL_HEADS):
                cols = slice(h * HEAD_DIM, (h + 1) * HEAD_DIM)
                qh = q_ref[rows, cols]
                k2, v2 = _dil_window(k_ref, kp_ref, i, cols), _dil_window(v_ref, vp_ref, i, cols)
                lh = l_ref[rows, h * HEAD_DIM:h * HEAD_DIM + 1]
                shift = l_ref[rows, h * HEAD_DIM + STAT_OFFSET:h * HEAD_DIM + STAT_OFFSET + 1]
                s = lax.dot_general(qh, k2, nt, preferred_element_type=F32) * ATTN_SCALE - (slopes[h] * dilation) * rel
                p = jnp.exp(jnp.where(valid, s, NEG_INF) - lh)
                dob = do_ref[rows, cols].astype(_CD)
                ds = p * (lax.dot_general(dob, v2, nt, preferred_element_type=F32) + shift)
                dsb = (ds * ATTN_SCALE).astype(_CD)
                dq_ref[rows, cols] = jnp.dot(dsb, k2, preferred_element_type=F32)
                dk_s[cols, window] += lax.dot_general(qh, dsb, tn, preferred_element_type=F32)
                dv_s[cols, window] += lax.dot_general(dob, p.astype(_CD), tn, preferred_element_type=F32)
        for i in range(4):
            rows = slice(i * DIL_W, (i + 1) * DIL_W)
            done = slice((i + 1) * DIL_W, (i + 2) * DIL_W)
            nat = _dil_rows(4 * n + i, S, dilation)
            dkb, dvb = dk_s[:, done].T, dv_s[:, done].T
            for half in range(DIL_OUT // LANES):
                cols = slice(half * LANES, (half + 1) * LANES)
                dqn_ref[half, nat, :] = dq_ref[rows, cols]
                dkn_ref[half, nat, :] = dkb[:, cols]
                dvn_ref[half, nat, :] = dvb[:, cols]
        dk_s[:, CHUNK:] = dk_s[:, :DIL_W]
        dv_s[:, CHUNK:] = dv_s[:, :DIL_W]

    def cur(which):
        return pl.BlockSpec((None, CHUNK, DIL_OUT), lambda s: (which, nchunk - 1 - s, 0))

    def prev(which):
        return pl.BlockSpec((None, DIL_W, DIL_OUT), lambda s: (which, jnp.maximum(4 * (nchunk - 1 - s) - 1, 0), 0))

    whole = pl.BlockSpec(SPLIT(S), lambda s: (0, 0, 0))
    shp = jax.ShapeDtypeStruct(SPLIT(S), F32)
    tile = pltpu.VMEM((CHUNK, DIL_OUT), F32)
    return _pcall(body, name=name, grid=(nchunk,),
                  in_specs=[cur(0), cur(1), cur(2), prev(1), prev(2), whole, whole],
                  out_specs=[whole, whole, whole], out_shape=[shp, shp, shp],
                  scratch_shapes=[pltpu.VMEM((DIL_OUT, CHUNK + DIL_W), F32), pltpu.VMEM((DIL_OUT, CHUNK + DIL_W), F32),
                                  tile, tile, tile],
                  compiler_params=pltpu.CompilerParams(dimension_semantics=("arbitrary",),
                                                       vmem_limit_bytes=VMEM_LIMIT_RESIDENT))(
        qkv, qkv, qkv, qkv, qkv, stats, do)


def _dil_mix_fwd(os_, ls_, *, name, tm=512):
    nh, S, _ = os_[0].shape

    def body(o0, o1, o2, l0, l1, l2, out_ref):
        for half in range(nh):
            ls = [l0[half], l1[half], l2[half]]
            m = jnp.maximum(jnp.maximum(ls[0], ls[1]), ls[2])
            es = [jnp.exp(l - m) for l in ls]
            den = es[0] + es[1] + es[2]
            mixed = (es[0] * o0[half] + es[1] * o1[half] + es[2] * o2[half]) / den
            out_ref[:, half * LANES:(half + 1) * LANES] = mixed.astype(out_ref.dtype)

    halves = pl.BlockSpec((nh, tm, LANES), lambda i: (0, i, 0))
    row = pl.BlockSpec((tm, nh * LANES), lambda i: (i, 0))
    return _pcall(body, name=name, grid=(S // tm,), in_specs=[halves] * 6, out_specs=row,
                  out_shape=jax.ShapeDtypeStruct((S, nh * LANES), _CD), compiler_params=_params("parallel"))(*os_, *ls_)


def _dil_mix_bwd(doa, os_, ls_, *, name, tm=512, after=None):
    nh, S, _ = os_[0].shape

    def body(d_ref, o0, o1, o2, l0, l1, l2, do0, do1, do2, st0, st1, st2):
        first = lax.broadcasted_iota(jnp.int32, (tm, LANES), 1) % HEAD_DIM < STAT_OFFSET
        for half in range(nh):
            dv = d_ref[:, half * LANES:(half + 1) * LANES]
            ls = [l0[half], l1[half], l2[half]]
            m = jnp.maximum(jnp.maximum(ls[0], ls[1]), ls[2])
            es = [jnp.exp(l - m) for l in ls]
            den = es[0] + es[1] + es[2]
            al = [e / den for e in es]
            da = [_head_sum(dv * o[half]) for o in (o0, o1, o2)]
            mean = al[0] * da[0] + al[1] * da[1] + al[2] * da[2]
            for a, l, do_ref, st_ref in zip(al, ls, (do0, do1, do2), (st0, st1, st2)):
                do_ref[half] = a * dv
                st_ref[half] = jnp.where(first, l, -a * mean)

    halves = pl.BlockSpec((nh, tm, LANES), lambda i: (0, i, 0))
    row = pl.BlockSpec((tm, nh * LANES), lambda i: (i, 0))
    shp = jax.ShapeDtypeStruct((nh, S, LANES), F32)
    return _pcall(body, after, name=name, grid=(S // tm,), in_specs=[row] + [halves] * 6, out_specs=[halves] * 6,
                  out_shape=[shp] * 6, compiler_params=_params("parallel"))(doa, *os_, *ls_)


FOX_T = 512


PACK = 2 * HEAD_DIM
HEAD_PAIRS = N_FOX_HEADS // 2
FOX_HPS = 8
Q_BLOCK0 = 0
K_BLOCK0 = FOX_WIDTH // PACK
V_BLOCK0 = 2 * FOX_WIDTH // PACK


def _pieces(x):
    hi = x.astype(jnp.bfloat16).astype(F32)
    r = x - hi
    mid = r.astype(jnp.bfloat16).astype(F32)
    lo = (r - mid).astype(jnp.bfloat16).astype(F32)
    return [hi, mid, lo]


def _extras(first, second, rows):
    lane = lax.broadcasted_iota(jnp.int32, (rows, HEAD_DIM), 1)
    out = jnp.zeros((rows, HEAD_DIM), F32)
    for base, triple in ((0, first), (3, second)):
        if all(isinstance(v, float) for v in triple) and len(set(triple)) == 1:
            if triple[0] != 0.0:
                out = jnp.where(jnp.logical_and(lane >= base, lane < base + 3), triple[0], out)
        else:
            for idx, val in enumerate(triple):
                out = jnp.where(lane == base + idx, val, out)
    return out


def _head_column(c, h):
    lane = lax.broadcasted_iota(jnp.int32, c.shape, 1)
    return jnp.sum(jnp.where(lane == h, c, 0.0), axis=1, keepdims=True)


ONES3 = [1.0, 1.0, 1.0]
ZEROS3 = [0.0, 0.0, 0.0]


def _fox_pack_fwd(qkv, c, *, name, tm=512):
    S = qkv.shape[0]

    def body(q_ref, k_ref, v_ref, c_ref, qo_ref, ko_ref, vo_ref):
        hp = pl.program_id(1)
        cv = c_ref[...]
        v_extras = jnp.where(lax.broadcasted_iota(jnp.int32, (tm, HEAD_DIM), 1) < 3, 1.0, 0.0).astype(vo_ref.dtype)
        for hh in range(2):
            ch = _pieces(_head_column(cv, 2 * hp + hh))
            src = slice(hh * HEAD_DIM, (hh + 1) * HEAD_DIM)
            lo = slice(hh * PACK, hh * PACK + HEAD_DIM)
            hi = slice(hh * PACK + HEAD_DIM, (hh + 1) * PACK)
            qo_ref[:, lo] = (q_ref[:, src].astype(F32) * ATTN_SCALE).astype(qo_ref.dtype)
            qo_ref[:, hi] = _extras(ch, ONES3, tm).astype(qo_ref.dtype)
            ko_ref[:, lo] = k_ref[:, src]
            ko_ref[:, hi] = _extras(ONES3, [-p for p in ch], tm).astype(ko_ref.dtype)
            vo_ref[:, lo] = v_ref[:, src]
            vo_ref[:, hi] = v_extras

    def src(block0):
        return pl.BlockSpec((tm, PACK), lambda i, hp: (i, block0 + hp))

    out = pl.BlockSpec((tm, 2 * PACK), lambda i, hp: (i, hp))
    shp = jax.ShapeDtypeStruct((S, N_FOX_HEADS * PACK), _CD)
    return _pcall(body, name=name, grid=(S // tm, HEAD_PAIRS),
                  in_specs=[src(Q_BLOCK0), src(K_BLOCK0), src(V_BLOCK0), pl.BlockSpec((tm, PACK), lambda i, hp: (i, 0))],
                  out_specs=[out, out, out], out_shape=[shp, shp, shp],
                  compiler_params=_params("parallel", "parallel"))(qkv, qkv, qkv, c)


def _fox_fwd(qp, kp, vp, *, name):
    S = qp.shape[0]
    nt = S // FOX_T
    nt_dims = (((1,), (1,)), ((), ()))
    tn_dims = (((0,), (0,)), ((), ()))

    def body(i_tab, j_tab, q_ref, k_ref, v_ref, o_ref, l_ref, m_s, acc_s):
        t = pl.program_id(1)
        i, j = i_tab[t], j_tab[t]

        @pl.when(j == 0)
        def _():
            m_s[...] = jnp.full((FOX_HPS, 1, FOX_T), NEG_INF, F32)
            acc_s[...] = jnp.zeros((FOX_HPS, PACK, FOX_T), F32)

        def tile(diagonal):
            for hh in range(FOX_HPS):
                cols = slice(hh * PACK, (hh + 1) * PACK)
                st = lax.dot_general(k_ref[:, cols], q_ref[:, cols], nt_dims, preferred_element_type=F32)
                if diagonal:
                    key = lax.broadcasted_iota(jnp.int32, (FOX_T, FOX_T), 0)
                    qry = lax.broadcasted_iota(jnp.int32, (FOX_T, FOX_T), 1)
                    st = jnp.where(key <= qry, st, NEG_INF)
                m_old = m_s[hh]
                m_new = jnp.maximum(m_old, jnp.max(st, axis=0, keepdims=True))
                pt = jnp.exp(st - m_new)
                acc_s[hh] = jnp.exp(m_old - m_new) * acc_s[hh] + lax.dot_general(
                    v_ref[:, cols], pt.astype(_CD), tn_dims, preferred_element_type=F32)
                m_s[hh] = m_new

        @pl.when(j < i)
        def _():
            tile(False)

        @pl.when(j == i)
        def _():
            tile(True)
            for hh in range(FOX_HPS):
                acc = acc_s[hh]
                den = acc[HEAD_DIM:HEAD_DIM + 1, :]
                cols = slice(hh * HEAD_DIM, (hh + 1) * HEAD_DIM)
                o_ref[:, cols] = (acc[:HEAD_DIM, :] / den).T
                l_ref[:, cols] = jnp.broadcast_to(m_s[hh] + jnp.log(den), (HEAD_DIM, FOX_T)).T

    pairs = [(i, j) for i in range(nt) for j in range(i + 1)]
    i_tab = jnp.asarray([p[0] for p in pairs], jnp.int32)
    j_tab = jnp.asarray([p[1] for p in pairs], jnp.int32)
    qs = pl.BlockSpec((FOX_T, FOX_HPS * PACK), lambda hp, t, it, jt: (it[t], hp))
    ks = pl.BlockSpec((FOX_T, FOX_HPS * PACK), lambda hp, t, it, jt: (jt[t], hp))
    os_ = pl.BlockSpec((FOX_T, FOX_HPS * HEAD_DIM), lambda hp, t, it, jt: (it[t], hp))
    shp = jax.ShapeDtypeStruct((S, FOX_WIDTH), F32)
    grid_spec = pltpu.PrefetchScalarGridSpec(
        num_scalar_prefetch=2, grid=(N_FOX_HEADS // FOX_HPS, len(pairs)), in_specs=[qs, ks, ks], out_specs=[os_, os_],
        scratch_shapes=[pltpu.VMEM((FOX_HPS, 1, FOX_T), F32), pltpu.VMEM((FOX_HPS, PACK, FOX_T), F32)])
    return _pcall(body, name=name, grid_spec=grid_spec, out_shape=[shp, shp],
                  compiler_params=_params("parallel", "arbitrary"))(i_tab, j_tab, qp, kp, vp)


def _fox_pack_bwd(qkv, c, o, lse, do, *, name, tm=512, after=None):
    S = qkv.shape[0]

    def body(q_ref, c_ref, o_ref, l_ref, do_ref, qo_ref, do_out_ref):
        hp = pl.program_id(1)
        cv = c_ref[...]
        for hh in range(2):
            src = slice(hh * HEAD_DIM, (hh + 1) * HEAD_DIM)
            lo = slice(hh * PACK, hh * PACK + HEAD_DIM)
            hi = slice(hh * PACK + HEAD_DIM, (hh + 1) * PACK)
            shift = _head_column(cv, 2 * hp + hh) - l_ref[:, hh * HEAD_DIM:hh * HEAD_DIM + 1]
            dov = do_ref[:, src]
            dsum = jnp.sum(dov * o_ref[:, src], axis=-1, keepdims=True)
            qo_ref[:, lo] = (q_ref[:, src].astype(F32) * ATTN_SCALE).astype(qo_ref.dtype)
            qo_ref[:, hi] = _extras(_pieces(shift), ONES3, tm).astype(qo_ref.dtype)
            do_out_ref[:, lo] = dov.astype(do_out_ref.dtype)
            do_out_ref[:, hi] = _extras(_pieces(-dsum), ZEROS3, tm).astype(do_out_ref.dtype)

    pair = pl.BlockSpec((tm, PACK), lambda i, hp: (i, hp))
    out = pl.BlockSpec((tm, 2 * PACK), lambda i, hp: (i, hp))
    shp = jax.ShapeDtypeStruct((S, N_FOX_HEADS * PACK), _CD)
    return _pcall(body, after, name=name, grid=(S // tm, HEAD_PAIRS),
                  in_specs=[pl.BlockSpec((tm, PACK), lambda i, hp: (i, Q_BLOCK0 + hp)),
                            pl.BlockSpec((tm, PACK), lambda i, hp: (i, 0)), pair, pair, pair],
                  out_specs=[out, out], out_shape=[shp, shp],
                  compiler_params=_params("parallel", "parallel"))(qkv, c, o, lse, do)


def _fox_bwd(qp, kp, vp, dop, *, name):
    S = qp.shape[0]
    nt = S // FOX_T
    nt_dims = (((1,), (1,)), ((), ()))
    tn_dims = (((0,), (0,)), ((), ()))

    def body(i_tab, j_tab, q_ref, k_ref, v_ref, do_ref, dq_ref, dk_ref, dv_ref, dc_ref, dr_ref,
             dq_s, dk_s, dv_s, dc_s, dr_s):
        t = pl.program_id(1)
        i, j = i_tab[t], j_tab[t]

        @pl.when(t == 0)
        def _():
            dq_s[...] = jnp.zeros((S, FOX_HPS * PACK), F32)
            dr_s[...] = jnp.zeros((FOX_HPS, 1, S), F32)

        @pl.when(i == j)
        def _():
            dk_s[...] = jnp.zeros((FOX_T, FOX_HPS * PACK), F32)
            dv_s[...] = jnp.zeros((FOX_T, FOX_HPS * PACK), F32)
            dc_s[...] = jnp.zeros((FOX_HPS, FOX_T, 1), F32)

        def tile(diagonal):
            rows = pl.ds(pl.multiple_of(i * FOX_T, FOX_T), FOX_T)
            for hh in range(FOX_HPS):
                cols = slice(hh * PACK, (hh + 1) * PACK)
                qv, kv, vv, dov = q_ref[:, cols], k_ref[:, cols], v_ref[:, cols], do_ref[:, cols]
                pt = jnp.exp(lax.dot_general(kv, qv, nt_dims, preferred_element_type=F32))
                if diagonal:
                    key = lax.broadcasted_iota(jnp.int32, (FOX_T, FOX_T), 0)
                    qry = lax.broadcasted_iota(jnp.int32, (FOX_T, FOX_T), 1)
                    pt = jnp.where(key <= qry, pt, 0.0)
                dst = pt * lax.dot_general(vv, dov, nt_dims, preferred_element_type=F32)
                dsb = dst.astype(_CD)
                dc_s[hh] += jnp.sum(dst, axis=1, keepdims=True)
                dr_s[hh, :, rows] += jnp.sum(dst, axis=0, keepdims=True)
                dv_s[:, cols] += jnp.dot(pt.astype(_CD), dov, preferred_element_type=F32)
                dk_s[:, cols] += jnp.dot(dsb, qv, preferred_element_type=F32)
                dq_s[rows, cols] += lax.dot_general(dsb, kv, tn_dims, preferred_element_type=F32)

        @pl.when(i > j)
        def _():
            tile(False)

        @pl.when(i == j)
        def _():
            tile(True)

        @pl.when(i == nt - 1)
        def _():
            for hh in range(FOX_HPS):
                src = slice(hh * PACK, hh * PACK + HEAD_DIM)
                dst_cols = slice(hh * HEAD_DIM, (hh + 1) * HEAD_DIM)
                dk_ref[:, dst_cols] = dk_s[:, src].astype(dk_ref.dtype)
                dv_ref[:, dst_cols] = dv_s[:, src].astype(dv_ref.dtype)
                dc_ref[:, dst_cols] = jnp.broadcast_to(dc_s[hh], (FOX_T, HEAD_DIM))

        @pl.when(t == len(pairs) - 1)
        def _():
            for hh in range(FOX_HPS):
                dq_ref[:, hh * HEAD_DIM:(hh + 1) * HEAD_DIM] = (
                    dq_s[:, hh * PACK:hh * PACK + HEAD_DIM] * ATTN_SCALE).astype(dq_ref.dtype)
            dr_ref[...] = dr_s[...]

    pairs = [(i, j) for j in range(nt) for i in range(j, nt)]
    i_tab = jnp.asarray([p[0] for p in pairs], jnp.int32)
    j_tab = jnp.asarray([p[1] for p in pairs], jnp.int32)
    wide, narrow = FOX_HPS * PACK, FOX_HPS * HEAD_DIM
    qs = pl.BlockSpec((FOX_T, wide), lambda hp, t, it, jt: (it[t], hp))
    ks = pl.BlockSpec((FOX_T, wide), lambda hp, t, it, jt: (jt[t], hp))
    whole = pl.BlockSpec((S, narrow), lambda hp, t, it, jt: (0, hp))
    cs = pl.BlockSpec((FOX_T, narrow), lambda hp, t, it, jt: (jt[t], hp))
    rs = pl.BlockSpec((FOX_HPS, 1, S), lambda hp, t, it, jt: (hp, 0, 0))
    shp = jax.ShapeDtypeStruct((S, FOX_WIDTH), _CD)
    grid_spec = pltpu.PrefetchScalarGridSpec(
        num_scalar_prefetch=2, grid=(N_FOX_HEADS // FOX_HPS, len(pairs)), in_specs=[qs, ks, ks, qs],
        out_specs=[whole, cs, cs, cs, rs],
        scratch_shapes=[pltpu.VMEM((S, wide), F32), pltpu.VMEM((FOX_T, wide), F32),
                        pltpu.VMEM((FOX_T, wide), F32), pltpu.VMEM((FOX_HPS, FOX_T, 1), F32),
                        pltpu.VMEM((FOX_HPS, 1, S), F32)])
    return _pcall(body, name=name, grid_spec=grid_spec,
                  out_shape=[shp, shp, shp, jax.ShapeDtypeStruct((S, FOX_WIDTH), F32),
                             jax.ShapeDtypeStruct((N_FOX_HEADS, 1, S), F32)],
                  compiler_params=_params("parallel", "arbitrary"))(i_tab, j_tab, qp, kp, vp, dop)


def _layer_step(x, tgt, w, p, late_weights=None, grad_sink=None, after=None, first_weights=None):
    S = x.shape[0]
    after_norm, after_proj = after if after is not None else (None, None)
    h = _rms_fwd(x, p["norm_mix_g"], name="rms_mix", after=after_norm)
    if first_weights is not None:
        w = {**w, **first_weights(h)}
    qkv = _mm(h, w["qkv"][:, 3 * DIL_WIDTH:], name="proj_fox", out_dtype=_CD, tn=768, tm=2048, after=after_proj)
    dil_qkv = _proj_dil(h, w["qkv"], name="proj_dil")
    zf = _mm(h, w["f"], name="proj_f")
    gl = _mm(h, w["g"], name="proj_gate", tn=1024, out_dtype=_CD)

    dil_o, dil_l = [], []
    for g in range(N_DIL_GROUPS):
        og, lg = _dil_fwd(dil_qkv[g], g, name=f"dil_fwd{g}")
        dil_o.append(og), dil_l.append(lg)
    o_a = _dil_mix_fwd(dil_o, dil_l, name="dil_mix")

    c = _fox_cumsum(zf, p["b_fgt"], name="fox_cumsum")
    fqp, fkp, fvp = _fox_pack_fwd(qkv, c, name="fox_pack")
    o_b, flse = _fox_fwd(fqp, fkp, fvp, name="fox_fwd")

    if late_weights is not None:
        w = {**w, **late_weights(o_b)}
    y_a = _mm(o_a, w["dil_out"], name="y_a", tn=1024, out_dtype=_CD)
    y_b = _mm(o_b, w["fox_out"], name="y_b", tn=1024, out_dtype=_CD)
    merged = _gate_fwd(gl, p["b_gate"], y_a, y_b, name="gate_fwd")
    x1 = _mm(merged, w["out"], name="mix_out", add=x)

    h2 = _rms_fwd(x1, p["norm_ffn_g"], name="rms_ffn")
    gate, up, act = _ffn_in_act(h2, w["ffn_in"], name="ffn_in")
    loss, dx2, dg_final = _ffn_down_loss(act, w["ffn_down"], x1, p["norm_final_g"], tgt, name="ffn_down_loss")

    gw_ffn_down = _mm(act, dx2, name="gw_ffn_down", ta=True, out_dtype=_CD, tm=1408)
    dgu = _d_swiglu(dx2, w["ffn_down"], gate, up, name="d_swiglu")
    gw_ffn_in = _mm(h2, dgu, name="gw_ffn_in", ta=True, out_dtype=_CD, tn=1408, out_blocks=1408, b_halves=True)
    sink = grad_sink if grad_sink is not None else (lambda group, grads: None)
    tok = sink("ffn", dict(ffn_in=gw_ffn_in, ffn_down=gw_ffn_down))
    dx1, dg_ffn = _mm(dgu, w["ffn_in"], name="d_h2", tb=True, tk=1408, b_blocks=True, tm=512, a_halves=True,
                      rms_bwd=(x1, p["norm_ffn_g"], dx2), after=tok)

    dmerged = _mm(dx1, w["out"], name="d_merged", tb=True, out_dtype=_CD)
    gw_out = _mm(merged, dx1, name="gw_out", ta=True, out_dtype=_CD)
    dy_a, dy_b, dgl, db_gate = _gate_bwd(dmerged, gl, p["b_gate"], y_a, y_b, name="gate_bwd")
    do_a = _mm(dy_a, w["dil_out"], name="d_o_a", tb=True)
    gw_dil_out = _mm(o_a, dy_a, name="gw_dil_out", ta=True, out_dtype=_CD, tn=1024)
    do_b = _mm(dy_b, w["fox_out"], name="d_o_b", tb=True)
    gw_fox_out = _mm(o_b, dy_b, name="gw_fox_out", ta=True, out_dtype=_CD, tn=1024)
    tok = sink("mix", dict(dil_out=gw_dil_out, fox_out=gw_fox_out, out=gw_out))

    bqp, bdop = _fox_pack_bwd(qkv, c, o_b, flse, do_b, name="fox_pack_bwd", after=tok)
    dqp, dkp, dvp, dck, dcq = _fox_bwd(bqp, fkp, fvp, bdop, name="fox_bwd")
    dc = dcq[:, 0, :].T - dck.reshape(S, N_FOX_HEADS, HEAD_DIM)[:, :, 0]
    dc = jnp.pad(dc, ((0, 0), (0, F_PAD - N_FOX_HEADS)))
    dzf, db_fgt = _fox_cumsum_bwd(dc, zf, p["b_fgt"], name="fox_cumsum_bwd")

    douts = _dil_mix_bwd(do_a, dil_o, dil_l, name="dil_mix_bwd", after=tok)
    dqs, dks, dvs = [], [], []
    for g in range(N_DIL_GROUPS):
        dq, dk, dv = _dil_bwd(dil_qkv[g], douts[3 + g], douts[g], g, name=f"dil_bwd{g}")
        for parts, t in ((dqs, dq), (dks, dk), (dvs, dv)):
            parts.extend([t[0].astype(_CD), t[1].astype(_CD)])
    dqkv = jnp.concatenate(dqs + dks + dvs + [dqp, dkp, dvp], axis=1)

    gw_qkv = _mm(h, dqkv, name="gw_qkv", ta=True, out_dtype=_CD, tn=768)
    gw_g = _mm(h, dgl, name="gw_gate", ta=True, out_dtype=_CD)
    gw_f = _mm(h, dzf, name="gw_f", ta=True, out_dtype=_CD)
    tok = sink("in", dict(qkv=gw_qkv, f=gw_f, g=gw_g))
    dh = _mm(dqkv, w["qkv"], name="d_h_qkv", tb=True, tk=1920, tm=2048, after=tok)
    dh = _mm(dgl, w["g"], name="d_h_gate", tb=True, add=dh)
    dx, dg_mix = _mm(dzf, w["f"], name="d_h_f", tb=True, add=dh, tm=512, rms_bwd=(x, p["norm_mix_g"], dx1))

    gw = dict(qkv=gw_qkv, f=gw_f, g=gw_g, dil_out=gw_dil_out, fox_out=gw_fox_out, out=gw_out, ffn_in=gw_ffn_in,
              ffn_down=gw_ffn_down)
    small = dict(norm_mix_g=dg_mix, b_fgt=db_fgt, b_gate=db_gate, norm_ffn_g=dg_ffn, norm_final_g=dg_final)
    return loss, dx, gw, small


def _position():
    return lax.axis_index("x"), lax.axis_index("y"), lax.axis_index("c")


def _other_chips(x, y):
    return [(1 - x, y), (x, 1 - y), (1 - x, 1 - y)]


ROW_TILE = 16


def _row_chunks(rows, want=4):
    n = want
    while n > 1 and rows % (n * ROW_TILE):
        n //= 2
    return n


SEM_SPEC = pl.BlockSpec(memory_space=pltpu.SEMAPHORE)
ANY_SPEC = pl.BlockSpec(memory_space=pl.ANY)
DATAFLOW = pltpu.SideEffectType.DATAFLOW_SIDE_EFFECTING


def _in_hbm(a):
    return pltpu.with_memory_space_constraint(a, pltpu.HBM)


def _split_copy_start(srcs, land_shapes, copies, after, *, name):
    n, m = len(srcs), len(land_shapes)

    def body(*refs):
        src_refs, land_refs = refs[:n], refs[n:n + m]
        send_sems, recv_sems = refs[n + m + 1], refs[n + m + 2]
        token = refs[-1]
        x, y, c = _position()
        for k, (src, dst, peer) in enumerate(copies(x, y, c, src_refs, land_refs)):
            pltpu.make_async_remote_copy(src_ref=src, dst_ref=dst, send_sem=send_sems.at[k], recv_sem=recv_sems.at[k],
                                         device_id=peer, device_id_type=MESH).start()
        token[...] = jnp.zeros_like(token)

    lands = [lax.empty(s.shape, s.dtype) for s in land_shapes]
    count = len(copies(0, 0, 0, srcs, lands))
    out = _pcall(
        body, name=name,
        out_shape=(pltpu.SemaphoreType.DMA((count,)), pltpu.SemaphoreType.DMA((count,)),
                   *[pltpu.HBM(s.shape, s.dtype) for s in srcs], *[pltpu.HBM(s.shape, s.dtype) for s in land_shapes],
                   jax.ShapeDtypeStruct((8, 128), F32)),
        in_specs=[HBM_SPEC] * (n + m) + [ANY_SPEC],
        out_specs=(SEM_SPEC, SEM_SPEC, *[HBM_SPEC] * (n + m), pl.BlockSpec(memory_space=pltpu.VMEM)),
        input_output_aliases={k: 2 + k for k in range(n + m)},
        compiler_params=pltpu.CompilerParams(has_side_effects=DATAFLOW),
    )(*[_in_hbm(s) for s in srcs], *[_in_hbm(l) for l in lands], after)
    return out[0], out[1], list(out[2:2 + n]), list(out[2 + n:2 + n + m]), out[-1]


def _split_copy_wait(send_sems, recv_sems, srcs, lands, copies, after, *, name):
    n, m = len(srcs), len(lands)

    def body(*refs):
        src_refs, land_refs = refs[:n], refs[n:n + m]
        send, recv = refs[n + m], refs[n + m + 1]
        x, y, c = _position()
        for k, (src, dst, peer) in enumerate(copies(x, y, c, src_refs, land_refs)):
            cp = pltpu.make_async_remote_copy(src_ref=src, dst_ref=dst, send_sem=send.at[k], recv_sem=recv.at[k],
                                              device_id=peer, device_id_type=MESH)
            cp.wait_send()
            cp.wait_recv()

    afters = list(after) if isinstance(after, (list, tuple)) else [after]
    out = _pcall(
        body, name=name,
        out_shape=tuple(pltpu.HBM(s.shape, s.dtype) for s in list(srcs) + list(lands)),
        in_specs=[HBM_SPEC] * (n + m) + [SEM_SPEC, SEM_SPEC] + [ANY_SPEC] * len(afters),
        out_specs=tuple([HBM_SPEC] * (n + m)),
        input_output_aliases={k: k for k in range(n + m)},
        compiler_params=pltpu.CompilerParams(has_side_effects=DATAFLOW),
    )(*srcs, *lands, send_sems, recv_sems, *afters)
    return list(out[:n]), list(out[n:])


def _gather_copies(x, y, c, shard_refs, land_refs):
    out = []
    for s, l in zip(shard_refs, land_refs):
        half = s.shape[0] // 2
        nq = _row_chunks(half)
        for cx, cy in _other_chips(x, y):
            for q in range(nq):
                rows = pl.ds(c * half + q * (half // nq), half // nq)
                out.append((s.at[rows, :], l.at[2 * x + y, rows, :], (cx, cy, c)))
    return out


def _gather_whole_copies(x, y, c, shard_refs, land_refs):
    out = []
    for s, l in zip(shard_refs, land_refs):
        nq = _row_chunks(s.shape[0])
        for cx, cy in _other_chips(x, y):
            for q in range(nq):
                rows = pl.ds(q * (s.shape[0] // nq), s.shape[0] // nq)
                out.append((s.at[rows, :], l.at[2 * x + y, rows, :], (cx, cy, c)))
    return out


def _scatter_all_copies(x, y, c, block_refs, land_refs):
    out = []
    for g, l in zip(block_refs, land_refs):
        half = g.shape[1] // 2
        nq = _row_chunks(half)
        size = half // nq
        for q in range(nq):
            rows = pl.ds((1 - c) * half + q * size, size)
            out.append((g.at[2 * x + y, rows, :], l.at[0, pl.ds(q * size, size), :], (x, y, 1 - c)))
        for r, (cx, cy) in enumerate(_other_chips(x, y)):
            for j in range(2):
                h = c if j == 0 else 1 - c
                for q in range(nq):
                    rows = pl.ds(h * half + q * size, size)
                    out.append((g.at[2 * cx + cy, rows, :], l.at[1 + 2 * r + j, pl.ds(q * size, size), :], (cx, cy, h)))
    return out


def _forward_halves(lands, *, name):
    n = len(lands)

    def body(*refs):
        ins = refs[:n]
        send_sems, recv_sems = refs[2 * n:]
        x, y, c = _position()
        copies = []
        for w in range(n):
            half = ins[w].shape[1] // 2
            for r, (cx, cy) in enumerate(_other_chips(x, y)):
                blk = ins[w].at[2 * cx + cy, pl.ds(c * half, half), :]
                cp = pltpu.make_async_remote_copy(src_ref=blk, dst_ref=blk, send_sem=send_sems.at[w, r],
                                                  recv_sem=recv_sems.at[w, r], device_id=(x, y, 1 - c),
                                                  device_id_type=MESH)
                cp.start()
                copies.append(cp)
        for w in range(n):
            half = ins[w].shape[1] // 2
            for r, (cx, cy) in enumerate(_other_chips(x, y)):
                blk = ins[w].at[2 * cx + cy, pl.ds((1 - c) * half, half), :]
                pltpu.make_async_remote_copy(src_ref=blk, dst_ref=blk, send_sem=send_sems.at[w, r],
                                             recv_sem=recv_sems.at[w, r], device_id=(x, y, 1 - c),
                                             device_id_type=MESH).wait_recv()
        for cp in copies:
            cp.wait_send()

    return _pcall(
        body, name=name, in_specs=[HBM_SPEC] * n, out_specs=[HBM_SPEC] * n,
        out_shape=[jax.ShapeDtypeStruct(l.shape, l.dtype) for l in lands],
        input_output_aliases={k: k for k in range(n)},
        scratch_shapes=[pltpu.SemaphoreType.DMA((n, 3)), pltpu.SemaphoreType.DMA((n, 3))],
    )(*lands)


def _share_halves(halves):
    n = len(halves)

    def body(*refs):
        ins, outs = refs[:n], refs[n:2 * n]
        send_sems, recv_sems = refs[2 * n:]
        x, y, c = _position()
        copies = []
        for w in range(n):
            cp = pltpu.make_async_remote_copy(src_ref=ins[w], dst_ref=outs[w], send_sem=send_sems.at[w],
                                              recv_sem=recv_sems.at[w], device_id=(x, y, 1 - c), device_id_type=MESH)
            cp.start()
            copies.append(cp)
        for cp in copies:
            cp.wait()

    return _pcall(
        body, name="share_halves", in_specs=[HBM_SPEC] * n, out_specs=[HBM_SPEC] * n,
        out_shape=[jax.ShapeDtypeStruct(h.shape, h.dtype) for h in halves],
        scratch_shapes=[pltpu.SemaphoreType.DMA((n,)), pltpu.SemaphoreType.DMA((n,))],
    )(*halves)


def _sum_small(part):
    rows, width = part.shape

    def body(x_ref, out_ref, all_ref, send_sems, recv_sems):
        x, y, c = _position()
        me, sibling = (x, y, c), (x, y, 1 - c)
        chips = _other_chips(x, y)

        def block(px, py, pc):
            return all_ref.at[pl.ds((4 * px + 2 * py + pc) * rows, rows), :]

        def copy(k, blk, to, src=None):
            return pltpu.make_async_remote_copy(
                src_ref=block(*blk) if src is None else src, dst_ref=block(*blk), send_sem=send_sems.at[k],
                recv_sem=recv_sems.at[k], device_id=to, device_id_type=MESH)

        all_ref[pl.ds((4 * x + 2 * y + c) * rows, rows), :] = x_ref[...]
        first = [copy(0, me, sibling, src=x_ref)]
        first += [copy(1 + j, me, (*chip, c), src=x_ref) for j, chip in enumerate(chips)]
        for cp in first:
            cp.start()
        passed = [copy(4 + j, (*chip, c), sibling) for j, chip in enumerate(chips)]
        for j, chip in enumerate(chips):
            copy(1 + j, (*chip, c), me).wait_recv()
            passed[j].start()
        copy(0, sibling, me).wait_recv()
        for j, chip in enumerate(chips):
            copy(4 + j, (*chip, 1 - c), me).wait_recv()
        for cp in first + passed:
            cp.wait_send()
        total = all_ref[0:rows, :]
        for d in range(1, 8):
            total = total + all_ref[d * rows:(d + 1) * rows, :]
        out_ref[...] = total

    vm = pl.BlockSpec(memory_space=pltpu.VMEM)
    return _pcall(
        body, name="sum_small", in_specs=[vm], out_specs=vm, out_shape=jax.ShapeDtypeStruct((rows, width), F32),
        scratch_shapes=[pltpu.VMEM((8 * rows, width), F32), pltpu.SemaphoreType.DMA((7,)), pltpu.SemaphoreType.DMA((7,))],
    )(part)


def _row_tile(R, C, itemsize=4, budget=1 << 20):
    for t in (512, 256, 128, 64, 32, 16, 8):
        if R % t == 0 and t * C * itemsize <= budget:
            return t
    return R


def _add_all(g, recv, where, *, name):
    _, R, C = g.shape
    half = R // 2
    t = _row_tile(half, C)
    nb = half // t

    def body(w_ref, g_ref, r_ref, o_ref):
        total = g_ref[0].astype(F32)
        for k in range(7):
            total = total + r_ref[k].astype(F32)
        o_ref[...] = total

    grid_spec = pltpu.PrefetchScalarGridSpec(
        num_scalar_prefetch=1, grid=(nb,),
        in_specs=[pl.BlockSpec((1, t, C), lambda i, wr: (wr[0], wr[1] * nb + i, 0)),
                  pl.BlockSpec((7, t, C), lambda i, wr: (0, i, 0))],
        out_specs=pl.BlockSpec((t, C), lambda i, wr: (i, 0)))
    return _pcall(body, name=name, grid_spec=grid_spec, out_shape=jax.ShapeDtypeStruct((half, C), F32),
                  compiler_params=_params("parallel"))(where, g, recv)


def _adamw(w, g, m, v, *, name):
    R, C = w.shape
    t = _row_tile(R, C)
    c1 = 1.0 - ADAM_B1 ** ADAM_STEP
    c2 = 1.0 - ADAM_B2 ** ADAM_STEP

    def body(w_ref, g_ref, m_ref, v_ref, d_ref, nm_ref, nv_ref):
        gv = g_ref[...]
        mn = ADAM_B1 * m_ref[...] + (1.0 - ADAM_B1) * gv
        vn = ADAM_B2 * v_ref[...] + (1.0 - ADAM_B2) * (gv * gv)
        d_ref[...] = -ADAM_LR * ((mn / c1) / (jnp.sqrt(vn / c2) + ADAM_EPS) + ADAM_WD * w_ref[...])
        nm_ref[...] = mn
        nv_ref[...] = vn

    blk = pl.BlockSpec((t, C), lambda i: (i, 0))
    shp = jax.ShapeDtypeStruct((R, C), F32)
    return _pcall(body, name=name, grid=(R // t,), in_specs=[blk] * 4, out_specs=[blk] * 3, out_shape=[shp] * 3,
                  compiler_params=_params("parallel"))(w, g, m, v)


def _adamw_halves(w, mine, theirs, m, v, core, *, name):
    R, C = w.shape
    half = R // 2
    t = _row_tile(half, C)
    nbh = half // t
    c1 = 1.0 - ADAM_B1 ** ADAM_STEP
    c2 = 1.0 - ADAM_B2 ** ADAM_STEP

    def body(core_ref, w_ref, a_ref, b_ref, m_ref, v_ref, g_ref, d_ref, nm_ref, nv_ref):
        gv = jnp.where(pl.program_id(0) // nbh == core_ref[0], a_ref[...], b_ref[...])
        mn = ADAM_B1 * m_ref[...] + (1.0 - ADAM_B1) * gv
        vn = ADAM_B2 * v_ref[...] + (1.0 - ADAM_B2) * (gv * gv)
        g_ref[...] = gv
        d_ref[...] = -ADAM_LR * ((mn / c1) / (jnp.sqrt(vn / c2) + ADAM_EPS) + ADAM_WD * w_ref[...])
        nm_ref[...] = mn
        nv_ref[...] = vn

    blk = pl.BlockSpec((t, C), lambda i, cr: (i, 0))
    hblk = pl.BlockSpec((t, C), lambda i, cr: (i % nbh, 0))
    shp = jax.ShapeDtypeStruct((R, C), F32)
    grid_spec = pltpu.PrefetchScalarGridSpec(num_scalar_prefetch=1, grid=(2 * nbh,),
                                             in_specs=[blk, hblk, hblk, blk, blk], out_specs=[blk] * 4)
    return _pcall(body, name=name, grid_spec=grid_spec, out_shape=[shp] * 4,
                  compiler_params=_params("parallel"))(core, w, mine, theirs, m, v)


BIG = ("w_in", "w_dil_out", "w_fox_out", "w_out", "w_ffn_in", "w_ffn_down")
SMALL = ("norm_mix_g", "b_fgt", "b_gate", "norm_ffn_g", "norm_final_g")
ORDER = ("norm_mix_g", "w_in", "b_fgt", "b_gate", "w_dil_out", "w_fox_out", "w_out", "norm_ffn_g", "w_ffn_in",
         "w_ffn_down", "norm_final_g")
SMALL_ROWS = {"norm_mix_g": (0, 1), "b_gate": (1, 3), "norm_ffn_g": (3, 4), "norm_final_g": (4, 5), "b_fgt": (5, 6)}


def _columns_to_blocks(full, ncol):
    K = full.shape[0]
    return full.reshape(K, 4, ncol).transpose(1, 0, 2)


def _blocks_to_columns(blocks):
    n, K, ncol = blocks.shape
    return blocks.transpose(1, 0, 2).reshape(K, n * ncol)


def kernel(x, norm_mix_g, w_in, b_fgt, b_gate, w_dil_out, w_fox_out, w_out, norm_ffn_g, w_ffn_in, w_ffn_down, norm_final_g, loss_target, m_norm_mix_g, m_w_in, m_b_fgt, m_b_gate, m_w_dil_out, m_w_fox_out, m_w_out, m_norm_ffn_g, m_w_ffn_in, m_w_ffn_down, m_norm_final_g, v_norm_mix_g, v_w_in, v_b_fgt, v_b_gate, v_w_dil_out, v_w_fox_out, v_w_out, v_norm_ffn_g, v_w_ffn_in, v_w_ffn_down, v_norm_final_g):
    weights = dict(norm_mix_g=norm_mix_g, w_in=w_in, b_fgt=b_fgt, b_gate=b_gate, w_dil_out=w_dil_out,
                   w_fox_out=w_fox_out, w_out=w_out, norm_ffn_g=norm_ffn_g, w_ffn_in=w_ffn_in, w_ffn_down=w_ffn_down,
                   norm_final_g=norm_final_g)
    m_in = dict(norm_mix_g=m_norm_mix_g, w_in=m_w_in, b_fgt=m_b_fgt, b_gate=m_b_gate, w_dil_out=m_w_dil_out,
                w_fox_out=m_w_fox_out, w_out=m_w_out, norm_ffn_g=m_norm_ffn_g, w_ffn_in=m_w_ffn_in,
                w_ffn_down=m_w_ffn_down, norm_final_g=m_norm_final_g)
    v_in = dict(norm_mix_g=v_norm_mix_g, w_in=v_w_in, b_fgt=v_b_fgt, b_gate=v_b_gate, w_dil_out=v_w_dil_out,
                w_fox_out=v_w_fox_out, w_out=v_w_out, norm_ffn_g=v_norm_ffn_g, w_ffn_in=v_w_ffn_in,
                w_ffn_down=v_w_ffn_down, norm_final_g=v_norm_final_g)
    c = lax.axis_index("c")
    chip = 2 * lax.axis_index("x") + lax.axis_index("y")

    shards = {n: weights[n][0].astype(_CD) for n in BIG}
    in_shape = jax.ShapeDtypeStruct((4,) + shards["w_in"].shape, _CD)
    send_i, recv_i, in_src, in_land, token_in = _split_copy_start(
        [shards["w_in"]], [in_shape], _gather_copies, norm_mix_g, name="gather_in_start")
    late = BIG[1:]
    send_g, recv_g, late_src, late_land, token = _split_copy_start(
        [shards[n] for n in late], [jax.ShapeDtypeStruct((4,) + shards[n].shape, _CD) for n in late],
        _gather_whole_copies, token_in, name="gather_late_start")
    adam_in = [t[0] + token_in[0, 0] for t in (w_in, m_w_in, v_w_in)]
    p = dict(norm_mix_g=norm_mix_g, b_fgt=jnp.pad(b_fgt, ((0, 0), (0, F_PAD - N_FOX_HEADS))), b_gate=b_gate,
             norm_ffn_g=norm_ffn_g, norm_final_g=norm_final_g.reshape(1, D_MODEL))

    def first_weights(after):
        own, lands = _split_copy_wait(send_i, recv_i, in_src, in_land, _gather_copies, [after] + adam_in,
                                      name="gather_in_wait")
        (g_in,) = _forward_halves(lands, name="gather_in_forward")
        full_in = _blocks_to_columns(lax.dynamic_update_index_in_dim(g_in, own[0], chip, 0))
        o3 = QKV_COLS
        o4 = o3 + N_FOX_HEADS
        return dict(qkv=full_in[:, :o3], f=jnp.pad(full_in[:, o3:o4], ((0, 0), (0, F_PAD - N_FOX_HEADS))),
                    g=full_in[:, o4:])

    def late_weights(after):
        own, lands = _split_copy_wait(send_g, recv_g, late_src, late_land, _gather_whole_copies, after,
                                      name="gather_late_wait")
        g_dil, g_fox, g_out, g_ffn_in, g_ffn_down = [
            lax.dynamic_update_index_in_dim(l, s, chip, 0) for l, s in zip(lands, own)]
        return dict(dil_out=_blocks_to_columns(g_dil), fox_out=_blocks_to_columns(g_fox),
                    out=g_out.reshape(D_MODEL, D_MODEL), ffn_in=g_ffn_in,
                    ffn_down=g_ffn_down.reshape(D_FF, D_MODEL))

    def to_blocks(n, full):
        shape = weights[n].shape
        if full.ndim == 3:
            return full
        if n in ("w_out", "w_ffn_down"):
            return full.reshape(4, shape[1], shape[2])
        return _columns_to_blocks(full, shape[2])

    in_flight = {}

    def grad_sink(group, gw):
        if group == "in":
            named = {"w_in": jnp.concatenate([gw["qkv"], gw["f"][:, :N_FOX_HEADS], gw["g"]], axis=1)}
        else:
            named = {"w_" + k: v for k, v in gw.items()}
        srcs = [to_blocks(n, named[n]) for n in named]
        lands = [jax.ShapeDtypeStruct((7, s.shape[1] // 2, s.shape[2]), s.dtype) for s in srcs]
        started = _split_copy_start(srcs, lands, _scatter_all_copies, next(iter(gw.values())),
                                    name=f"scatter_{group}_start")
        in_flight[group] = (list(named), started)
        return started[-1]

    loss_part, grad_x, gw, small = _layer_step(x[0], loss_target[0], {}, p, late_weights, grad_sink,
                                               (token_in, token), first_weights)

    halves = {}
    where = jnp.stack([chip, c]).astype(jnp.int32)
    for group, (names, (send_s, recv_s, srcs, lands, _)) in in_flight.items():
        srcs, recv = _split_copy_wait(send_s, recv_s, srcs, lands, _scatter_all_copies, grad_x,
                                      name=f"scatter_{group}_wait")
        halves.update({n: _add_all(s, r, where, name=f"add_all_{n}") for n, s, r in zip(names, srcs, recv)})
    halves = [halves[n] for n in BIG]
    grad_halves = dict(zip(BIG, zip(halves, _share_halves(halves))))
    grads = {}

    packed = jnp.concatenate([
        small["norm_mix_g"], small["b_gate"].reshape(2, D_MODEL), small["norm_ffn_g"], small["norm_final_g"],
        jnp.pad(small["b_fgt"], ((0, 0), (0, D_MODEL - F_PAD))), jnp.pad(loss_part, ((0, 0), (0, D_MODEL - 1))),
        jnp.zeros((1, D_MODEL), F32)], axis=0)
    summed = _sum_small(packed)
    for n in SMALL:
        lo, hi = SMALL_ROWS[n]
        grads[n] = summed[lo:hi].reshape(1, -1)[:, :weights[n].size]
    loss = summed[6, 0]

    out_g, out_d, out_m, out_v = {}, {}, {}, {}
    core = jnp.reshape(c, (1,)).astype(jnp.int32)
    for n in ORDER:
        shape = weights[n].shape
        two_d = shape[1:] if len(shape) == 3 else (1, weights[n].size)
        wmv = adam_in if n == "w_in" else [t.reshape(two_d) for t in (weights[n], m_in[n], v_in[n])]
        if n in grad_halves:
            mine, theirs = grad_halves[n]
            g2, d2, m2, v2 = _adamw_halves(wmv[0], mine, theirs, wmv[1], wmv[2], core, name=f"adamw_{n}")
        else:
            g2 = grads[n].reshape(two_d)
            d2, m2, v2 = _adamw(wmv[0], g2, wmv[1], wmv[2], name=f"adamw_{n}")
        out_g[n], out_d[n], out_m[n], out_v[n] = (g2.reshape(shape), d2.reshape(shape), m2.reshape(shape),
                                                  v2.reshape(shape))
    return (loss, grad_x[None], *[out_g[n] for n in ORDER], *[out_d[n] for n in ORDER],
            *[out_m[n] for n in ORDER], *[out_v[n] for n in ORDER])
```

```python
import numpy as np
import jax
import jax.numpy as jnp
from jax import lax
from jax.experimental import pallas as pl
from jax.experimental.pallas import tpu as pltpu

F32 = jnp.float32
_CD = jnp.bfloat16

D_MODEL = 1024
HEAD_DIM = 64
DIL_PAIRS = ((128, 1), (512, 4), (2048, 16))
N_DIL_GROUPS = 3
DIL_HEADS = 4
DIL_W = 128
DIL_OUT = DIL_HEADS * HEAD_DIM
DIL_WIDTH = N_DIL_GROUPS * DIL_OUT
N_FOX_HEADS = 8
FOX_WIDTH = N_FOX_HEADS * HEAD_DIM
D_FF = 2816
QKV_COLS = 3 * DIL_WIDTH + 3 * FOX_WIDTH
F_PAD = 128
RMS_EPS = 1e-6
NEG_INF = -1e30
ATTN_SCALE = HEAD_DIM ** -0.5
ADAM_LR, ADAM_B1, ADAM_B2, ADAM_EPS, ADAM_WD, ADAM_STEP = 0.001, 0.9, 0.999, 1e-08, 0.01, 10

VMEM_LIMIT = 48 * 1024 * 1024
VMEM_LIMIT_RESIDENT = 56 * 1024 * 1024
LANES = 128
MESH = pl.DeviceIdType.MESH
HBM_SPEC = pl.BlockSpec(memory_space=pltpu.HBM)


def _pcall(body, after=None, **kw):
    if after is None:
        return pl.pallas_call(body, **kw)
    n_in = len(kw["in_specs"])
    kw["in_specs"] = list(kw["in_specs"]) + [pl.BlockSpec(memory_space=pl.ANY)]

    def tied(*refs):
        return body(*refs[:n_in], *refs[n_in + 1:])

    call = pl.pallas_call(tied, **kw)
    return lambda *args: call(*args, after)


def _params(*sem):
    return pltpu.CompilerParams(dimension_semantics=sem, vmem_limit_bytes=VMEM_LIMIT)


def _pick(dim, pref):
    t = (min(pref, dim) // 128) * 128
    while t >= 128:
        if dim % t == 0:
            return t
        t -= 128
    return dim


def _mm(a, b, *, name, ta=False, tb=False, out_dtype=F32, add=None, tm=1024, tn=512, tk=2048, after=None,
        b_blocks=False, out_blocks=None, a_halves=False, b_halves=False, rms_bwd=None, rms_fwd=None):
    if a_halves:
        M, K = a.shape[1], 2 * a.shape[2]
    elif ta:
        K, M = a.shape
    else:
        M, K = a.shape
    if b_halves:
        b_rows, b_cols = b.shape[1], 2 * b.shape[2]
    else:
        b_rows, b_cols = (b.shape[1], b.shape[0] * b.shape[2]) if b_blocks else b.shape
    if tb:
        N, K2 = b_rows, b_cols
    else:
        K2, N = b_rows, b_cols
    assert K == K2, (a.shape, b.shape)
    shard = b.shape[2] if b_blocks else None
    tm = _pick(M, tm)
    tn = _pick(shard if (b_blocks and not tb) else (out_blocks or N), tn)
    tk = _pick(shard if (b_blocks and tb) else K, tk)
    nk = K // tk
    dn = (((0 if ta else 1,), (1 if tb else 0,)), ((), ()))
    has_add = add is not None
    assert not (has_add and out_blocks)
    has_norm = rms_bwd is not None
    has_next = rms_fwd is not None
    if has_norm or has_next:
        tn = N
        assert not out_blocks and out_dtype == F32 and not (has_norm and has_next)

    def body(*refs):
        a_ref, b_ref = refs[0], refs[1]
        rest = list(refs[2:])
        add_ref = rest.pop(0) if has_add else None
        x_ref, g_ref, dres_ref = (rest.pop(0), rest.pop(0), rest.pop(0)) if has_norm else (None, None, None)
        gain_ref = rest.pop(0) if has_next else None
        o_ref = rest.pop(0)
        dg_ref = rest.pop(0) if has_norm else None
        h_ref = rest.pop(0) if has_next else None
        bv = b_ref[0] if b_blocks else b_ref[...]
        p = lax.dot_general(a_ref[...].astype(_CD), bv.astype(_CD), dn, preferred_element_type=F32)

        def finish(r):
            if has_add:
                r = r + add_ref[...]
            if has_norm:
                xv = x_ref[...]
                rs = lax.rsqrt(jnp.mean(xv * xv, axis=-1, keepdims=True) + RMS_EPS)
                xh = xv * rs
                dxh = r * g_ref[...]
                o_ref[...] = dres_ref[...] + rs * (dxh - xh * jnp.mean(dxh * xh, axis=-1, keepdims=True))
                part = jnp.sum(r * xh, axis=0, keepdims=True)
                first = pl.program_id(0) == 0

                @pl.when(first)
                def _():
                    dg_ref[...] = part

                @pl.when(jnp.logical_not(first))
                def _():
                    dg_ref[...] += part
            elif has_next:
                o_ref[...] = r
                rs = lax.rsqrt(jnp.mean(r * r, axis=-1, keepdims=True) + RMS_EPS)
                h_ref[...] = ((r * rs) * gain_ref[...]).astype(h_ref.dtype)
            elif out_blocks:
                o_ref[0] = r.astype(out_dtype)
            else:
                o_ref[...] = r.astype(out_dtype)

        if nk == 1:
            finish(p)
        else:
            acc_ref = rest.pop(0)
            k = pl.program_id(2)

            @pl.when(k == 0)
            def _():
                acc_ref[...] = p

            @pl.when(k > 0)
            def _():
                acc_ref[...] += p

            @pl.when(k == nk - 1)
            def _():
                finish(acc_ref[...])

    if a_halves:
        ka = (K // 2) // tk
        a_spec = pl.BlockSpec((None, tm, tk), lambda i, j, k: (k // ka, i, k % ka))
    else:
        a_spec = pl.BlockSpec((tk, tm), lambda i, j, k: (k, i)) if ta else pl.BlockSpec((tm, tk), lambda i, j, k: (i, k))
    if b_halves:
        nb_ = (N // 2) // tn
        b_spec = pl.BlockSpec((None, tk, tn), lambda i, j, k: (j // nb_, k, j % nb_))
    elif b_blocks and tb:
        per = shard // tk
        b_spec = pl.BlockSpec((1, tn, tk), lambda i, j, k: (k // per, j, k % per))
    elif b_blocks:
        per = shard // tn
        b_spec = pl.BlockSpec((1, tk, tn), lambda i, j, k: (j // per, k, j % per))
    else:
        b_spec = pl.BlockSpec((tn, tk), lambda i, j, k: (j, k)) if tb else pl.BlockSpec((tk, tn), lambda i, j, k: (k, j))
    if out_blocks:
        oper = out_blocks // tn
        o_spec = pl.BlockSpec((1, tm, tn), lambda i, j, k: (j // oper, i, j % oper))
        out_shape = jax.ShapeDtypeStruct((N // out_blocks, M, out_blocks), out_dtype)
    else:
        o_spec = pl.BlockSpec((tm, tn), lambda i, j, k: (i, j))
        out_shape = jax.ShapeDtypeStruct((M, N), out_dtype)
    in_specs = [a_spec, b_spec] + ([o_spec] if has_add else [])
    args = (a, b) + ((add,) if has_add else ())
    out_specs, semantics = o_spec, ("parallel", "parallel", "arbitrary")
    if has_norm:
        vec = pl.BlockSpec((1, N), lambda i, j, k: (0, 0))
        in_specs += [o_spec, vec, o_spec]
        args += tuple(rms_bwd)
        out_specs, out_shape = [o_spec, vec], [out_shape, jax.ShapeDtypeStruct((1, N), F32)]
        semantics = ("arbitrary", "arbitrary", "arbitrary")
    if has_next:
        in_specs += [pl.BlockSpec((1, N), lambda i, j, k: (0, 0))]
        args += (rms_fwd,)
        out_specs, out_shape = [o_spec, o_spec], [out_shape, jax.ShapeDtypeStruct((M, N), _CD)]
    return _pcall(
        body, after, name=name, grid=(M // tm, N // tn, nk), in_specs=in_specs, out_specs=out_specs,
        out_shape=out_shape,
        scratch_shapes=[pltpu.VMEM((tm, tn), F32)] if nk > 1 else [],
        compiler_params=_params(*semantics),
    )(*args)


def _rms_fwd(x, g, *, name, tm=512, after=None):
    S, D = x.shape

    def body(x_ref, g_ref, h_ref):
        xv = x_ref[...]
        r = lax.rsqrt(jnp.mean(xv * xv, axis=-1, keepdims=True) + RMS_EPS)
        h_ref[...] = ((xv * r) * g_ref[...]).astype(h_ref.dtype)

    row = pl.BlockSpec((tm, D), lambda i: (i, 0))
    return _pcall(body, after, name=name, grid=(S // tm,), in_specs=[row, pl.BlockSpec((1, D), lambda i: (0, 0))],
                  out_specs=row, out_shape=jax.ShapeDtypeStruct((S, D), _CD), compiler_params=_params("parallel"))(x, g)


def _ffn_down_loss(act, w_down, x1, g, tgt, *, name, tm=512):
    S, D = x1.shape
    F = act.shape[1]

    def body(a_ref, b_ref, x_ref, g_ref, t_ref, loss_ref, dx_ref, dg_ref):
        xv = x_ref[...] + jnp.dot(a_ref[...].astype(_CD), b_ref[...].astype(_CD), preferred_element_type=F32)
        gv = g_ref[...]
        r = lax.rsqrt(jnp.mean(xv * xv, axis=-1, keepdims=True) + RMS_EPS)
        xh = xv * r
        err = xh * gv - t_ref[...]
        lpart = 0.5 * jnp.sum(jnp.mean(err * err, axis=-1, keepdims=True), axis=0, keepdims=True)
        dy = err * (1.0 / D)
        dxh = dy * gv
        dx_ref[...] = r * (dxh - xh * jnp.mean(dxh * xh, axis=-1, keepdims=True))
        gpart = jnp.sum(dy * xh, axis=0, keepdims=True)

        @pl.when(pl.program_id(0) == 0)
        def _():
            loss_ref[...] = lpart
            dg_ref[...] = gpart

        @pl.when(pl.program_id(0) > 0)
        def _():
            loss_ref[...] += lpart
            dg_ref[...] += gpart

    row = pl.BlockSpec((tm, D), lambda i: (i, 0))
    vec = pl.BlockSpec((1, D), lambda i: (0, 0))
    one = pl.BlockSpec((1, 1), lambda i: (0, 0))
    return _pcall(body, name=name, grid=(S // tm,),
                  in_specs=[pl.BlockSpec((tm, F), lambda i: (i, 0)), pl.BlockSpec((F, D), lambda i: (0, 0)), row, vec, row],
                  out_specs=[one, row, vec],
                  out_shape=[jax.ShapeDtypeStruct((1, 1), F32), jax.ShapeDtypeStruct((S, D), F32),
                             jax.ShapeDtypeStruct((1, D), F32)],
                  compiler_params=_params("arbitrary"))(act, w_down, x1, g, tgt)


def _sigmoid(z):
    return 1.0 / (1.0 + jnp.exp(-z))


def _gate_fwd(gl, bg, ya, yb, *, name, tm=512):
    S, D = ya.shape

    def body(za_ref, zb_ref, ba_ref, bb_ref, ya_ref, yb_ref, o_ref):
        ga = _sigmoid(za_ref[...].astype(F32) + ba_ref[...])
        gb = _sigmoid(zb_ref[...].astype(F32) + bb_ref[...])
        o_ref[...] = (ga * ya_ref[...].astype(F32) + gb * yb_ref[...].astype(F32)).astype(o_ref.dtype)

    lo = pl.BlockSpec((tm, D), lambda i: (i, 0))
    hi = pl.BlockSpec((tm, D), lambda i: (i, 1))
    vlo = pl.BlockSpec((1, D), lambda i: (0, 0))
    vhi = pl.BlockSpec((1, D), lambda i: (0, 1))
    return _pcall(body, name=name, grid=(S // tm,), in_specs=[lo, hi, vlo, vhi, lo, lo], out_specs=lo,
                  out_shape=jax.ShapeDtypeStruct((S, D), _CD), compiler_params=_params("parallel"))(gl, gl, bg, bg, ya, yb)


def _gate_bwd(dm, gl, bg, ya, yb, *, name, tm=256):
    S, D = ya.shape

    def body(dm_ref, za_ref, zb_ref, ba_ref, bb_ref, ya_ref, yb_ref, dya_ref, dyb_ref, dgl_ref, dbg_ref):
        dmv = dm_ref[...].astype(F32)
        ga = _sigmoid(za_ref[...].astype(F32) + ba_ref[...])
        gb = _sigmoid(zb_ref[...].astype(F32) + bb_ref[...])
        dya_ref[...] = (dmv * ga).astype(dya_ref.dtype)
        dyb_ref[...] = (dmv * gb).astype(dyb_ref.dtype)
        dza = dmv * ya_ref[...].astype(F32) * ga * (1.0 - ga)
        dzb = dmv * yb_ref[...].astype(F32) * gb * (1.0 - gb)
        dgl_ref[:, :D] = dza.astype(dgl_ref.dtype)
        dgl_ref[:, D:] = dzb.astype(dgl_ref.dtype)
        pa = jnp.sum(dza, axis=0, keepdims=True)
        pb = jnp.sum(dzb, axis=0, keepdims=True)

        @pl.when(pl.program_id(0) == 0)
        def _():
            dbg_ref[:, :D] = pa
            dbg_ref[:, D:] = pb

        @pl.when(pl.program_id(0) > 0)
        def _():
            dbg_ref[:, :D] += pa
            dbg_ref[:, D:] += pb

    lo = pl.BlockSpec((tm, D), lambda i: (i, 0))
    hi = pl.BlockSpec((tm, D), lambda i: (i, 1))
    vlo = pl.BlockSpec((1, D), lambda i: (0, 0))
    vhi = pl.BlockSpec((1, D), lambda i: (0, 1))
    wide = pl.BlockSpec((tm, 2 * D), lambda i: (i, 0))
    vwide = pl.BlockSpec((1, 2 * D), lambda i: (0, 0))
    return _pcall(body, name=name, grid=(S // tm,), in_specs=[lo, lo, hi, vlo, vhi, lo, lo],
                  out_specs=[lo, lo, wide, vwide],
                  out_shape=[jax.ShapeDtypeStruct((S, D), _CD), jax.ShapeDtypeStruct((S, D), _CD),
                             jax.ShapeDtypeStruct((S, 2 * D), _CD), jax.ShapeDtypeStruct((1, 2 * D), F32)],
                  compiler_params=_params("arbitrary"))(dm, gl, gl, bg, bg, ya, yb)


def _ffn_in_act(h2, w_blocks, *, name, tm=512):
    S, D = h2.shape
    _, _, C = w_blocks.shape

    def body(a_ref, bg_ref, bu_ref, g_ref, u_ref, o_ref):
        av = a_ref[...].astype(_CD)
        gv = jnp.dot(av, bg_ref[0].astype(_CD), preferred_element_type=F32)
        uv = jnp.dot(av, bu_ref[0].astype(_CD), preferred_element_type=F32)
        g_ref[...] = gv.astype(g_ref.dtype)
        u_ref[...] = uv.astype(u_ref.dtype)
        o_ref[...] = (gv * _sigmoid(gv) * uv).astype(o_ref.dtype)

    out = pl.BlockSpec((tm, C), lambda i, j: (i, j))
    shp = jax.ShapeDtypeStruct((S, 2 * C), _CD)
    return _pcall(body, name=name, grid=(S // tm, 2),
                  in_specs=[pl.BlockSpec((tm, D), lambda i, j: (i, 0)), pl.BlockSpec((1, D, C), lambda i, j: (j, 0, 0)),
                            pl.BlockSpec((1, D, C), lambda i, j: (2 + j, 0, 0))],
                  out_specs=[out, out, out], out_shape=[shp, shp, shp],
                  compiler_params=_params("parallel", "arbitrary"))(h2, w_blocks, w_blocks)


def _d_swiglu(dx, w_down, gate, up, *, name, tm=512, tn=1408):
    S, D = dx.shape
    F = w_down.shape[0]
    nt = (((1,), (1,)), ((), ()))

    def body(a_ref, b_ref, g_ref, u_ref, o_ref):
        dv = lax.dot_general(a_ref[...].astype(_CD), b_ref[...].astype(_CD), nt, preferred_element_type=F32)
        gv = g_ref[...].astype(F32)
        sg = _sigmoid(gv)
        o_ref[0] = (dv * u_ref[...].astype(F32) * (sg * (1.0 + gv * (1.0 - sg)))).astype(o_ref.dtype)
        o_ref[1] = (dv * (gv * sg)).astype(o_ref.dtype)

    tile = pl.BlockSpec((tm, tn), lambda i, j: (i, j))
    return _pcall(body, name=name, grid=(S // tm, F // tn),
                  in_specs=[pl.BlockSpec((tm, D), lambda i, j: (i, 0)), pl.BlockSpec((tn, D), lambda i, j: (j, 0)),
                            tile, tile],
                  out_specs=pl.BlockSpec((2, tm, tn), lambda i, j: (0, i, j)),
                  out_shape=jax.ShapeDtypeStruct((2, S, F), _CD),
                  compiler_params=_params("parallel", "arbitrary"))(dx, w_down, gate, up)


def _split3(x):
    hi = x.astype(jnp.bfloat16)
    r1 = x - hi.astype(F32)
    mid = r1.astype(jnp.bfloat16)
    lo = (r1 - mid.astype(F32)).astype(jnp.bfloat16)
    return hi, mid, lo


def _ones_dot_left(ones, x):
    return sum(jnp.dot(ones, p, preferred_element_type=F32) for p in _split3(x))


def _ones_dot_right(x, ones):
    return sum(jnp.dot(p, ones, preferred_element_type=F32) for p in _split3(x))


def _head_sum(x):
    n = x.shape[1]
    r = lax.broadcasted_iota(jnp.int32, (n, n), 0) // HEAD_DIM
    c = lax.broadcasted_iota(jnp.int32, (n, n), 1) // HEAD_DIM
    return _ones_dot_right(x, (r == c).astype(jnp.bfloat16))


def _log_sigmoid(z):
    e = jnp.exp(-jnp.abs(z))
    t = 1.0 + e
    log1p_e = jnp.where(t == 1.0, e, jnp.log(t) * (e / jnp.where(t == 1.0, 1.0, t - 1.0)))
    return jnp.minimum(z, 0.0) - log1p_e


def _fox_cumsum(zf, bf, *, name):
    S, W = zf.shape
    nb = S // 128

    def body(z_ref, b_ref, c_ref):
        tri = (lax.broadcasted_iota(jnp.int32, (128, 128), 0) >= lax.broadcasted_iota(jnp.int32, (128, 128), 1))
        tri = tri.astype(jnp.bfloat16)

        def step(i, carry):
            rows = pl.ds(pl.multiple_of(i * 128, 128), 128)
            lf = _log_sigmoid(z_ref[rows, :] + b_ref[...])
            cb = _ones_dot_left(tri, lf) + carry
            c_ref[rows, :] = cb
            return cb[127:128, :]

        lax.fori_loop(0, nb, step, jnp.zeros((1, W), F32))

    return _pcall(body, name=name, out_shape=jax.ShapeDtypeStruct((S, W), F32),
                  compiler_params=pltpu.CompilerParams(vmem_limit_bytes=VMEM_LIMIT))(zf, bf)


def _fox_cumsum_bwd(dc, zf, bf, *, name):
    S, W = zf.shape
    nb = S // 128

    def body(dc_ref, z_ref, b_ref, dz_ref, db_ref):
        tri = (lax.broadcasted_iota(jnp.int32, (128, 128), 0) <= lax.broadcasted_iota(jnp.int32, (128, 128), 1))
        tri = tri.astype(jnp.bfloat16)

        def step(k, carry):
            tail, acc = carry
            i = nb - 1 - k
            rows = pl.ds(pl.multiple_of(i * 128, 128), 128)
            dlf = _ones_dot_left(tri, dc_ref[rows, :]) + tail
            dz = dlf * _sigmoid(-(z_ref[rows, :] + b_ref[...]))
            dz_ref[rows, :] = dz
            return dlf[0:1, :], acc + jnp.sum(dz, axis=0, keepdims=True)

        _, acc = lax.fori_loop(0, nb, step, (jnp.zeros((1, W), F32), jnp.zeros((1, W), F32)))
        db_ref[...] = acc

    return _pcall(body, name=name,
                  out_shape=[jax.ShapeDtypeStruct((S, W), F32), jax.ShapeDtypeStruct((1, W), F32)],
                  compiler_params=pltpu.CompilerParams(vmem_limit_bytes=VMEM_LIMIT))(dc, zf, bf)


def _proj_dil(h, w_qkv, *, name, tm=1024):
    S, D = h.shape
    tn = DIL_WIDTH

    def body(a_ref, b_ref, *rest):
        outs, acc = rest[:N_DIL_GROUPS], rest[N_DIL_GROUPS]
        prod = jnp.dot(a_ref[...].astype(_CD), b_ref[...].astype(_CD), preferred_element_type=F32)
        for k in range(tn // LANES):
            acc[k] = prod[:, k * LANES:(k + 1) * LANES]
        for g, (_, d) in enumerate(DIL_PAIRS):
            for half in range(DIL_OUT // LANES):
                k = g * (DIL_OUT // LANES) + half
                cols = slice(half * LANES, (half + 1) * LANES)
                for r in range(d):
                    rows = pl.ds(r, tm // d, stride=d) if d > 1 else slice(None)
                    outs[g][0, r, :, cols] = acc[k, rows, :].astype(outs[g].dtype)

    out_specs = [pl.BlockSpec((1, d, tm // d, DIL_OUT), lambda i, j: (j, 0, i, 0)) for _, d in DIL_PAIRS]
    out_shape = [jax.ShapeDtypeStruct((3, d, S // d, DIL_OUT), _CD) for _, d in DIL_PAIRS]
    outs = _pcall(body, name=name, grid=(S // tm, 3),
                  in_specs=[pl.BlockSpec((tm, D), lambda i, j: (i, 0)), pl.BlockSpec((D, tn), lambda i, j: (0, j))],
                  out_specs=out_specs, out_shape=out_shape, scratch_shapes=[pltpu.VMEM((tn // LANES, tm, LANES), F32)],
                  compiler_params=_params("parallel", "arbitrary"))(h, w_qkv)
    return [o.reshape(3, S, DIL_OUT) for o in outs]


def _dil_start(block, S, dilation):
    sub = S // dilation
    u0 = block * DIL_W
    return (u0 % sub) * dilation + u0 // sub


def _dil_slopes(group):
    h = np.arange(1, N_DIL_GROUPS * DIL_HEADS + 1, dtype=np.float32)
    s = (np.float32(2.0) ** (np.float32(-8.0) * h / np.float32(N_DIL_GROUPS * DIL_HEADS))).astype(np.float32)
    return [float(v) for v in s.reshape(N_DIL_GROUPS, DIL_HEADS)[group]]


def _dil_tiles(i, n, blocks_per_seq):
    qi = lax.broadcasted_iota(jnp.int32, (DIL_W, 2 * DIL_W), 0)
    kj = lax.broadcasted_iota(jnp.int32, (DIL_W, 2 * DIL_W), 1)
    rel = qi + DIL_W - kj
    first = ((4 * n + i) % blocks_per_seq) == 0
    valid = jnp.logical_and(jnp.logical_and(rel >= 0, rel <= DIL_W), jnp.logical_or(kj >= DIL_W, jnp.logical_not(first)))
    return valid, rel.astype(F32)


def _dil_window(cur_ref, prev_ref, i, cols):
    if i > 0:
        return cur_ref[(i - 1) * DIL_W:(i + 1) * DIL_W, cols]
    return jnp.concatenate([prev_ref[:, cols], cur_ref[:DIL_W, cols]], axis=0)


CHUNK = 4 * DIL_W


def _dil_rows(block, S, dilation):
    start = _dil_start(block, S, dilation)
    return pl.ds(start, DIL_W, stride=dilation) if dilation > 1 else pl.ds(start, DIL_W)


def SPLIT(S):
    return (DIL_OUT // LANES, S, LANES)


def _dil_fwd(qkv, group, *, name):
    S = qkv.shape[1]
    dilation = DIL_PAIRS[group][1]
    bps = (S // dilation) // DIL_W
    slopes = _dil_slopes(group)
    nt = (((1,), (1,)), ((), ()))

    def body(q_ref, k_ref, v_ref, kp_ref, vp_ref, on_ref, ln_ref, o_ref, l_ref):
        n = pl.program_id(0)
        for i in range(4):
            valid, rel = _dil_tiles(i, n, bps)
            rows = slice(i * DIL_W, (i + 1) * DIL_W)
            for h in range(DIL_HEADS):
                cols = slice(h * HEAD_DIM, (h + 1) * HEAD_DIM)
                qh = q_ref[rows, cols]
                k2, v2 = _dil_window(k_ref, kp_ref, i, cols), _dil_window(v_ref, vp_ref, i, cols)
                s = lax.dot_general(qh, k2, nt, preferred_element_type=F32) * ATTN_SCALE - (slopes[h] * dilation) * rel
                s = jnp.where(valid, s, NEG_INF)
                m = jnp.max(s, axis=-1, keepdims=True)
                p = jnp.exp(s - m)
                den = jnp.sum(p, axis=-1, keepdims=True)
                acc = jnp.dot(p.astype(_CD), v2, preferred_element_type=F32)
                o_ref[rows, cols] = acc / den
                l_ref[rows, cols] = jnp.broadcast_to(m + jnp.log(den), (DIL_W, HEAD_DIM))
        for i in range(4):
            rows = slice(i * DIL_W, (i + 1) * DIL_W)
            nat = _dil_rows(4 * n + i, S, dilation)
            for half in range(DIL_OUT // LANES):
                cols = slice(half * LANES, (half + 1) * LANES)
                on_ref[half, nat, :] = o_ref[rows, cols]
                ln_ref[half, nat, :] = l_ref[rows, cols]

    def cur(which):
        return pl.BlockSpec((None, CHUNK, DIL_OUT), lambda n: (which, n, 0))

    def prev(which):
        return pl.BlockSpec((None, DIL_W, DIL_OUT), lambda n: (which, jnp.maximum(4 * n - 1, 0), 0))

    whole = pl.BlockSpec(SPLIT(S), lambda n: (0, 0, 0))
    return _pcall(body, name=name, grid=(S // CHUNK,), in_specs=[cur(0), cur(1), cur(2), prev(1), prev(2)],
                  out_specs=[whole, whole],
                  out_shape=[jax.ShapeDtypeStruct(SPLIT(S), F32), jax.ShapeDtypeStruct(SPLIT(S), F32)],
                  scratch_shapes=[pltpu.VMEM((CHUNK, DIL_OUT), F32), pltpu.VMEM((CHUNK, DIL_OUT), F32)],
                  compiler_params=_params("arbitrary"))(qkv, qkv, qkv, qkv, qkv)


STAT_OFFSET = HEAD_DIM // 2


def _dil_bwd(qkv, stats, do, group, *, name):
    S = qkv.shape[1]
    dilation = DIL_PAIRS[group][1]
    bps = (S // dilation) // DIL_W
    slopes = _dil_slopes(group)
    nchunk = S // CHUNK
    nt = (((1,), (1,)), ((), ()))
    tn = (((0,), (0,)), ((), ()))

    def body(q_ref, k_ref, v_ref, kp_ref, vp_ref, ln_ref, don_ref, dqn_ref, dkn_ref, dvn_ref,
             dk_s, dv_s, l_ref, do_ref, dq_ref):
        step = pl.program_id(0)
        n = nchunk - 1 - step
        for i in range(4):
            rows = slice(i * DIL_W, (i + 1) * DIL_W)
            nat = _dil_rows(4 * n + i, S, dilation)
            for half in range(DIL_OUT // LANES):
                cols = slice(half * LANES, (half + 1) * LANES)
                l_ref[rows, cols] = ln_ref[half, nat, :]
                do_ref[rows, cols] = don_ref[half, nat, :]

        @pl.when(step == 0)
        def _():
            dk_s[:, CHUNK:] = jnp.zeros((DIL_OUT, DIL_W), F32)
            dv_s[:, CHUNK:] = jnp.zeros((DIL_OUT, DIL_W), F32)

        dk_s[:, :CHUNK] = jnp.zeros((DIL_OUT, CHUNK), F32)
        dv_s[:, :CHUNK] = jnp.zeros((DIL_OUT, CHUNK), F32)
        for i in range(4):
            valid, rel = _dil_tiles(i, n, bps)
            rows = slice(i * DIL_W, (i + 1) * DIL_W)
            window = slice(i * DIL_W, (i + 2) * DIL_W)
            for h in range(DIL_HEADS):
                cols = slice(h * HEAD_DIM, (h + 1) * HEAD_DIM)
                qh = q_ref[rows, cols]
                k2, v2 = _dil_window(k_ref, kp_ref, i, cols), _dil_window(v_ref, vp_ref, i, cols)
                lh = l_ref[rows, h * HEAD_DIM:h * HEAD_DIM + 1]
                shift = l_ref[rows, h * HEAD_DIM + STAT_OFFSET:h * HEAD_DIM + STAT_OFFSET + 1]
                s = lax.dot_general(qh, k2, nt, preferred_element_type=F32) * ATTN_SCALE - (slopes[h] * dilation) * rel
                p = jnp.exp(jnp.where(valid, s, NEG_INF) - lh)
                dob = do_ref[rows, cols].astype(_CD)
                ds = p * (lax.dot_general(dob, v2, nt, preferred_element_type=F32) + shift)
                dsb = (ds * ATTN_SCALE).astype(_CD)
                dq_ref[rows, cols] = jnp.dot(dsb, k2, preferred_element_type=F32)
                dk_s[cols, window] += lax.dot_general(qh, dsb, tn, preferred_element_type=F32)
                dv_s[cols, window] += lax.dot_general(dob, p.astype(_CD), tn, preferred_element_type=F32)
        for i in range(4):
            rows = slice(i * DIL_W, (i + 1) * DIL_W)
            done = slice((i + 1) * DIL_W, (i + 2) * DIL_W)
            nat = _dil_rows(4 * n + i, S, dilation)
            dkb, dvb = dk_s[:, done].T, dv_s[:, done].T
            for half in range(DIL_OUT // LANES):
                cols = slice(half * LANES, (half + 1) * LANES)
                dqn_ref[half, nat, :] = dq_ref[rows, cols]
                dkn_ref[half, nat, :] = dkb[:, cols]
                dvn_ref[half, nat, :] = dvb[:, cols]
        dk_s[:, CHUNK:] = dk_s[:, :DIL_W]
        dv_s[:, CHUNK:] = dv_s[:, :DIL_W]

    def cur(which):
        return pl.BlockSpec((None, CHUNK, DIL_OUT), lambda s: (which, nchunk - 1 - s, 0))

    def prev(which):
        return pl.BlockSpec((None, DIL_W, DIL_OUT), lambda s: (which, jnp.maximum(4 * (nchunk - 1 - s) - 1, 0), 0))

    whole = pl.BlockSpec(SPLIT(S), lambda s: (0, 0, 0))
    shp = jax.ShapeDtypeStruct(SPLIT(S), F32)
    tile = pltpu.VMEM((CHUNK, DIL_OUT), F32)
    return _pcall(body, name=name, grid=(nchunk,),
                  in_specs=[cur(0), cur(1), cur(2), prev(1), prev(2), whole, whole],
                  out_specs=[whole, whole, whole], out_shape=[shp, shp, shp],
                  scratch_shapes=[pltpu.VMEM((DIL_OUT, CHUNK + DIL_W), F32), pltpu.VMEM((DIL_OUT, CHUNK + DIL_W), F32),
                                  tile, tile, tile],
                  compiler_params=pltpu.CompilerParams(dimension_semantics=("arbitrary",),
                                                       vmem_limit_bytes=VMEM_LIMIT_RESIDENT))(
        qkv, qkv, qkv, qkv, qkv, stats, do)


def _dil_mix_fwd(os_, ls_, *, name, tm=512):
    nh, S, _ = os_[0].shape

    def body(o0, o1, o2, l0, l1, l2, out_ref):
        for half in range(nh):
            ls = [l0[half], l1[half], l2[half]]
            m = jnp.maximum(jnp.maximum(ls[0], ls[1]), ls[2])
            es = [jnp.exp(l - m) for l in ls]
            den = es[0] + es[1] + es[2]
            mixed = (es[0] * o0[half] + es[1] * o1[half] + es[2] * o2[half]) / den
            out_ref[:, half * LANES:(half + 1) * LANES] = mixed.astype(out_ref.dtype)

    halves = pl.BlockSpec((nh, tm, LANES), lambda i: (0, i, 0))
    row = pl.BlockSpec((tm, nh * LANES), lambda i: (i, 0))
    return _pcall(body, name=name, grid=(S // tm,), in_specs=[halves] * 6, out_specs=row,
                  out_shape=jax.ShapeDtypeStruct((S, nh * LANES), _CD), compiler_params=_params("parallel"))(*os_, *ls_)


def _dil_mix_bwd(doa, os_, ls_, *, name, tm=512, after=None):
    nh, S, _ = os_[0].shape

    def body(d_ref, o0, o1, o2, l0, l1, l2, do0, do1, do2, st0, st1, st2):
        first = lax.broadcasted_iota(jnp.int32, (tm, LANES), 1) % HEAD_DIM < STAT_OFFSET
        for half in range(nh):
            dv = d_ref[:, half * LANES:(half + 1) * LANES]
            ls = [l0[half], l1[half], l2[half]]
            m = jnp.maximum(jnp.maximum(ls[0], ls[1]), ls[2])
            es = [jnp.exp(l - m) for l in ls]
            den = es[0] + es[1] + es[2]
            al = [e / den for e in es]
            da = [_head_sum(dv * o[half]) for o in (o0, o1, o2)]
            mean = al[0] * da[0] + al[1] * da[1] + al[2] * da[2]
            for a, l, do_ref, st_ref in zip(al, ls, (do0, do1, do2), (st0, st1, st2)):
                do_ref[half] = a * dv
                st_ref[half] = jnp.where(first, l, -a * mean)

    halves = pl.BlockSpec((nh, tm, LANES), lambda i: (0, i, 0))
    row = pl.BlockSpec((tm, nh * LANES), lambda i: (i, 0))
    shp = jax.ShapeDtypeStruct((nh, S, LANES), F32)
    return _pcall(body, after, name=name, grid=(S // tm,), in_specs=[row] + [halves] * 6, out_specs=[halves] * 6,
                  out_shape=[shp] * 6, compiler_params=_params("parallel"))(doa, *os_, *ls_)


FOX_T = 512


PACK = 2 * HEAD_DIM
HEAD_PAIRS = N_FOX_HEADS // 2
FOX_HPS = 8
Q_BLOCK0 = 0
K_BLOCK0 = FOX_WIDTH // PACK
V_BLOCK0 = 2 * FOX_WIDTH // PACK


def _pieces(x):
    hi = x.astype(jnp.bfloat16).astype(F32)
    r = x - hi
    mid = r.astype(jnp.bfloat16).astype(F32)
    lo = (r - mid).astype(jnp.bfloat16).astype(F32)
    return [hi, mid, lo]


def _extras(first, second, rows):
    lane = lax.broadcasted_iota(jnp.int32, (rows, HEAD_DIM), 1)
    out = jnp.zeros((rows, HEAD_DIM), F32)
    for base, triple in ((0, first), (3, second)):
        if all(isinstance(v, float) for v in triple) and len(set(triple)) == 1:
            if triple[0] != 0.0:
                out = jnp.where(jnp.logical_and(lane >= base, lane < base + 3), triple[0], out)
        else:
            for idx, val in enumerate(triple):
                out = jnp.where(lane == base + idx, val, out)
    return out


def _head_column(c, h):
    lane = lax.broadcasted_iota(jnp.int32, c.shape, 1)
    return jnp.sum(jnp.where(lane == h, c, 0.0), axis=1, keepdims=True)


ONES3 = [1.0, 1.0, 1.0]
ZEROS3 = [0.0, 0.0, 0.0]


def _fox_pack_fwd(qkv, c, *, name, tm=512):
    S = qkv.shape[0]

    def body(q_ref, k_ref, v_ref, c_ref, qo_ref, ko_ref, vo_ref):
        hp = pl.program_id(1)
        cv = c_ref[...]
        v_extras = jnp.where(lax.broadcasted_iota(jnp.int32, (tm, HEAD_DIM), 1) < 3, 1.0, 0.0).astype(vo_ref.dtype)
        for hh in range(2):
            ch = _pieces(_head_column(cv, 2 * hp + hh))
            src = slice(hh * HEAD_DIM, (hh + 1) * HEAD_DIM)
            lo = slice(hh * PACK, hh * PACK + HEAD_DIM)
            hi = slice(hh * PACK + HEAD_DIM, (hh + 1) * PACK)
            qo_ref[:, lo] = (q_ref[:, src].astype(F32) * ATTN_SCALE).astype(qo_ref.dtype)
            qo_ref[:, hi] = _extras(ch, ONES3, tm).astype(qo_ref.dtype)
            ko_ref[:, lo] = k_ref[:, src]
            ko_ref[:, hi] = _extras(ONES3, [-p for p in ch], tm).astype(ko_ref.dtype)
            vo_ref[:, lo] = v_ref[:, src]
            vo_ref[:, hi] = v_extras

    def src(block0):
        return pl.BlockSpec((tm, PACK), lambda i, hp: (i, block0 + hp))

    out = pl.BlockSpec((tm, 2 * PACK), lambda i, hp: (i, hp))
    shp = jax.ShapeDtypeStruct((S, N_FOX_HEADS * PACK), _CD)
    return _pcall(body, name=name, grid=(S // tm, HEAD_PAIRS),
                  in_specs=[src(Q_BLOCK0), src(K_BLOCK0), src(V_BLOCK0), pl.BlockSpec((tm, PACK), lambda i, hp: (i, 0))],
                  out_specs=[out, out, out], out_shape=[shp, shp, shp],
                  compiler_params=_params("parallel", "parallel"))(qkv, qkv, qkv, c)


def _fox_fwd(qp, kp, vp, *, name):
    S = qp.shape[0]
    nt = S // FOX_T
    nt_dims = (((1,), (1,)), ((), ()))
    tn_dims = (((0,), (0,)), ((), ()))

    def body(i_tab, j_tab, q_ref, k_ref, v_ref, o_ref, l_ref, m_s, acc_s):
        t = pl.program_id(1)
        i, j = i_tab[t], j_tab[t]

        @pl.when(j == 0)
        def _():
            m_s[...] = jnp.full((FOX_HPS, 1, FOX_T), NEG_INF, F32)
            acc_s[...] = jnp.zeros((FOX_HPS, PACK, FOX_T), F32)

        def tile(diagonal):
            for hh in range(FOX_HPS):
                cols = slice(hh * PACK, (hh + 1) * PACK)
                st = lax.dot_general(k_ref[:, cols], q_ref[:, cols], nt_dims, preferred_element_type=F32)
                if diagonal:
                    key = lax.broadcasted_iota(jnp.int32, (FOX_T, FOX_T), 0)
                    qry = lax.broadcasted_iota(jnp.int32, (FOX_T, FOX_T), 1)
                    st = jnp.where(key <= qry, st, NEG_INF)
                m_old = m_s[hh]
                m_new = jnp.maximum(m_old, jnp.max(st, axis=0, keepdims=True))
                pt = jnp.exp(st - m_new)
                acc_s[hh] = jnp.exp(m_old - m_new) * acc_s[hh] + lax.dot_general(
                    v_ref[:, cols], pt.astype(_CD), tn_dims, preferred_element_type=F32)
                m_s[hh] = m_new

        @pl.when(j < i)
        def _():
            tile(False)

        @pl.when(j == i)
        def _():
            tile(True)
            for hh in range(FOX_HPS):
                acc = acc_s[hh]
                den = acc[HEAD_DIM:HEAD_DIM + 1, :]
                cols = slice(hh * HEAD_DIM, (hh + 1) * HEAD_DIM)
                o_ref[:, cols] = (acc[:HEAD_DIM, :] / den).T
                l_ref[:, cols] = jnp.broadcast_to(m_s[hh] + jnp.log(den), (HEAD_DIM, FOX_T)).T

    pairs = [(i, j) for i in range(nt) for j in range(i + 1)]
    i_tab = jnp.asarray([p[0] for p in pairs], jnp.int32)
    j_tab = jnp.asarray([p[1] for p in pairs], jnp.int32)
    qs = pl.BlockSpec((FOX_T, FOX_HPS * PACK), lambda hp, t, it, jt: (it[t], hp))
    ks = pl.BlockSpec((FOX_T, FOX_HPS * PACK), lambda hp, t, it, jt: (jt[t], hp))
    os_ = pl.BlockSpec((FOX_T, FOX_HPS * HEAD_DIM), lambda hp, t, it, jt: (it[t], hp))
    shp = jax.ShapeDtypeStruct((S, FOX_WIDTH), F32)
    grid_spec = pltpu.PrefetchScalarGridSpec(
        num_scalar_prefetch=2, grid=(N_FOX_HEADS // FOX_HPS, len(pairs)), in_specs=[qs, ks, ks], out_specs=[os_, os_],
        scratch_shapes=[pltpu.VMEM((FOX_HPS, 1, FOX_T), F32), pltpu.VMEM((FOX_HPS, PACK, FOX_T), F32)])
    return _pcall(body, name=name, grid_spec=grid_spec, out_shape=[shp, shp],
                  compiler_params=_params("parallel", "arbitrary"))(i_tab, j_tab, qp, kp, vp)


def _fox_pack_bwd(qkv, c, o, lse, do, *, name, tm=512, after=None):
    S = qkv.shape[0]

    def body(q_ref, c_ref, o_ref, l_ref, do_ref, qo_ref, do_out_ref):
        hp = pl.program_id(1)
        cv = c_ref[...]
        for hh in range(2):
            src = slice(hh * HEAD_DIM, (hh + 1) * HEAD_DIM)
            lo = slice(hh * PACK, hh * PACK + HEAD_DIM)
            hi = slice(hh * PACK + HEAD_DIM, (hh + 1) * PACK)
            shift = _head_column(cv, 2 * hp + hh) - l_ref[:, hh * HEAD_DIM:hh * HEAD_DIM + 1]
            dov = do_ref[:, src]
            dsum = jnp.sum(dov * o_ref[:, src], axis=-1, keepdims=True)
            qo_ref[:, lo] = (q_ref[:, src].astype(F32) * ATTN_SCALE).astype(qo_ref.dtype)
            qo_ref[:, hi] = _extras(_pieces(shift), ONES3, tm).astype(qo_ref.dtype)
            do_out_ref[:, lo] = dov.astype(do_out_ref.dtype)
            do_out_ref[:, hi] = _extras(_pieces(-dsum), ZEROS3, tm).astype(do_out_ref.dtype)

    pair = pl.BlockSpec((tm, PACK), lambda i, hp: (i, hp))
    out = pl.BlockSpec((tm, 2 * PACK), lambda i, hp: (i, hp))
    shp = jax.ShapeDtypeStruct((S, N_FOX_HEADS * PACK), _CD)
    return _pcall(body, after, name=name, grid=(S // tm, HEAD_PAIRS),
                  in_specs=[pl.BlockSpec((tm, PACK), lambda i, hp: (i, Q_BLOCK0 + hp)),
                            pl.BlockSpec((tm, PACK), lambda i, hp: (i, 0)), pair, pair, pair],
                  out_specs=[out, out], out_shape=[shp, shp],
                  compiler_params=_params("parallel", "parallel"))(qkv, c, o, lse, do)


def _fox_bwd(qp, kp, vp, dop, *, name):
    S = qp.shape[0]
    nt = S // FOX_T
    nt_dims = (((1,), (1,)), ((), ()))
    tn_dims = (((0,), (0,)), ((), ()))

    def body(i_tab, j_tab, q_ref, k_ref, v_ref, do_ref, dq_ref, dk_ref, dv_ref, dc_ref, dr_ref,
             dq_s, dk_s, dv_s, dc_s, dr_s):
        t = pl.program_id(1)
        i, j = i_tab[t], j_tab[t]

        @pl.when(t == 0)
        def _():
            dq_s[...] = jnp.zeros((S, FOX_HPS * PACK), F32)
            dr_s[...] = jnp.zeros((FOX_HPS, 1, S), F32)

        @pl.when(i == j)
        def _():
            dk_s[...] = jnp.zeros((FOX_T, FOX_HPS * PACK), F32)
            dv_s[...] = jnp.zeros((FOX_T, FOX_HPS * PACK), F32)
            dc_s[...] = jnp.zeros((FOX_HPS, FOX_T, 1), F32)

        def tile(diagonal):
            rows = pl.ds(pl.multiple_of(i * FOX_T, FOX_T), FOX_T)
            for hh in range(FOX_HPS):
                cols = slice(hh * PACK, (hh + 1) * PACK)
                qv, kv, vv, dov = q_ref[:, cols], k_ref[:, cols], v_ref[:, cols], do_ref[:, cols]
                pt = jnp.exp(lax.dot_general(kv, qv, nt_dims, preferred_element_type=F32))
                if diagonal:
                    key = lax.broadcasted_iota(jnp.int32, (FOX_T, FOX_T), 0)
                    qry = lax.broadcasted_iota(jnp.int32, (FOX_T, FOX_T), 1)
                    pt = jnp.where(key <= qry, pt, 0.0)
                dst = pt * lax.dot_general(vv, dov, nt_dims, preferred_element_type=F32)
                dsb = dst.astype(_CD)
                dc_s[hh] += jnp.sum(dst, axis=1, keepdims=True)
                dr_s[hh, :, rows] += jnp.sum(dst, axis=0, keepdims=True)
                dv_s[:, cols] += jnp.dot(pt.astype(_CD), dov, preferred_element_type=F32)
                dk_s[:, cols] += jnp.dot(dsb, qv, preferred_element_type=F32)
                dq_s[rows, cols] += lax.dot_general(dsb, kv, tn_dims, preferred_element_type=F32)

        @pl.when(i > j)
        def _():
            tile(False)

        @pl.when(i == j)
        def _():
            tile(True)

        @pl.when(i == nt - 1)
        def _():
            for hh in range(FOX_HPS):
                src = slice(hh * PACK, hh * PACK + HEAD_DIM)
                dst_cols = slice(hh * HEAD_DIM, (hh + 1) * HEAD_DIM)
                dk_ref[:, dst_cols] = dk_s[:, src].astype(dk_ref.dtype)
                dv_ref[:, dst_cols] = dv_s[:, src].astype(dv_ref.dtype)
                dc_ref[:, dst_cols] = jnp.broadcast_to(dc_s[hh], (FOX_T, HEAD_DIM))

        @pl.when(t == len(pairs) - 1)
        def _():
            for hh in range(FOX_HPS):
                dq_ref[:, hh * HEAD_DIM:(hh + 1) * HEAD_DIM] = (
                    dq_s[:, hh * PACK:hh * PACK + HEAD_DIM] * ATTN_SCALE).astype(dq_ref.dtype)
            dr_ref[...] = dr_s[...]

    pairs = [(i, j) for j in range(nt) for i in range(j, nt)]
    i_tab = jnp.asarray([p[0] for p in pairs], jnp.int32)
    j_tab = jnp.asarray([p[1] for p in pairs], jnp.int32)
    wide, narrow = FOX_HPS * PACK, FOX_HPS * HEAD_DIM
    qs = pl.BlockSpec((FOX_T, wide), lambda hp, t, it, jt: (it[t], hp))
    ks = pl.BlockSpec((FOX_T, wide), lambda hp, t, it, jt: (jt[t], hp))
    whole = pl.BlockSpec((S, narrow), lambda hp, t, it, jt: (0, hp))
    cs = pl.BlockSpec((FOX_T, narrow), lambda hp, t, it, jt: (jt[t], hp))
    rs = pl.BlockSpec((FOX_HPS, 1, S), lambda hp, t, it, jt: (hp, 0, 0))
    shp = jax.ShapeDtypeStruct((S, FOX_WIDTH), _CD)
    grid_spec = pltpu.PrefetchScalarGridSpec(
        num_scalar_prefetch=2, grid=(N_FOX_HEADS // FOX_HPS, len(pairs)), in_specs=[qs, ks, ks, qs],
        out_specs=[whole, cs, cs, cs, rs],
        scratch_shapes=[pltpu.VMEM((S, wide), F32), pltpu.VMEM((FOX_T, wide), F32),
                        pltpu.VMEM((FOX_T, wide), F32), pltpu.VMEM((FOX_HPS, FOX_T, 1), F32),
                        pltpu.VMEM((FOX_HPS, 1, S), F32)])
    return _pcall(body, name=name, grid_spec=grid_spec,
                  out_shape=[shp, shp, shp, jax.ShapeDtypeStruct((S, FOX_WIDTH), F32),
                             jax.ShapeDtypeStruct((N_FOX_HEADS, 1, S), F32)],
                  compiler_params=_params("parallel", "arbitrary"))(i_tab, j_tab, qp, kp, vp, dop)


def _layer_step(x, tgt, w, p, late_weights=None, grad_sink=None, after=None, first_weights=None):
    S = x.shape[0]
    after_norm, after_proj = after if after is not None else (None, None)
    h = _rms_fwd(x, p["norm_mix_g"], name="rms_mix", after=after_norm)
    if first_weights is not None:
        w = {**w, **first_weights(h)}
    qkv = _mm(h, w["qkv"][:, 3 * DIL_WIDTH:], name="proj_fox", out_dtype=_CD, tn=768, tm=2048, after=after_proj)
    dil_qkv = _proj_dil(h, w["qkv"], name="proj_dil")
    zf = _mm(h, w["f"], name="proj_f")
    gl = _mm(h, w["g"], name="proj_gate", tn=1024, out_dtype=_CD)

    dil_o, dil_l = [], []
    for g in range(N_DIL_GROUPS):
        og, lg = _dil_fwd(dil_qkv[g], g, name=f"dil_fwd{g}")
        dil_o.append(og), dil_l.append(lg)
    o_a = _dil_mix_fwd(dil_o, dil_l, name="dil_mix")

    c = _fox_cumsum(zf, p["b_fgt"], name="fox_cumsum")
    fqp, fkp, fvp = _fox_pack_fwd(qkv, c, name="fox_pack")
    o_b, flse = _fox_fwd(fqp, fkp, fvp, name="fox_fwd")

    if late_weights is not None:
        w = {**w, **late_weights(o_b)}
    y_a = _mm(o_a, w["dil_out"], name="y_a", tn=1024, out_dtype=_CD)
    y_b = _mm(o_b, w["fox_out"], name="y_b", tn=1024, out_dtype=_CD)
    merged = _gate_fwd(gl, p["b_gate"], y_a, y_b, name="gate_fwd")
    x1, h2 = _mm(merged, w["out"], name="mix_out", add=x, rms_fwd=p["norm_ffn_g"])
    gate, up, act = _ffn_in_act(h2, w["ffn_in"], name="ffn_in")
    loss, dx2, dg_final = _ffn_down_loss(act, w["ffn_down"], x1, p["norm_final_g"], tgt, name="ffn_down_loss")

    gw_ffn_down = _mm(act, dx2, name="gw_ffn_down", ta=True, out_dtype=_CD, tm=1408)
    dgu = _d_swiglu(dx2, w["ffn_down"], gate, up, name="d_swiglu")
    gw_ffn_in = _mm(h2, dgu, name="gw_ffn_in", ta=True, out_dtype=_CD, tn=1408, out_blocks=1408, b_halves=True)
    sink = grad_sink if grad_sink is not None else (lambda group, grads: None)
    tok = sink("ffn", dict(ffn_in=gw_ffn_in, ffn_down=gw_ffn_down))
    dx1, dg_ffn = _mm(dgu, w["ffn_in"], name="d_h2", tb=True, tk=1408, b_blocks=True, tm=1024, a_halves=True,
                      rms_bwd=(x1, p["norm_ffn_g"], dx2), after=tok)

    dmerged = _mm(dx1, w["out"], name="d_merged", tb=True, out_dtype=_CD)
    gw_out = _mm(merged, dx1, name="gw_out", ta=True, out_dtype=_CD)
    dy_a, dy_b, dgl, db_gate = _gate_bwd(dmerged, gl, p["b_gate"], y_a, y_b, name="gate_bwd")
    do_a = _mm(dy_a, w["dil_out"], name="d_o_a", tb=True)
    gw_dil_out = _mm(o_a, dy_a, name="gw_dil_out", ta=True, out_dtype=_CD, tn=1024)
    do_b = _mm(dy_b, w["fox_out"], name="d_o_b", tb=True)
    gw_fox_out = _mm(o_b, dy_b, name="gw_fox_out", ta=True, out_dtype=_CD, tn=1024)
    tok = sink("mix", dict(dil_out=gw_dil_out, fox_out=gw_fox_out, out=gw_out))

    bqp, bdop = _fox_pack_bwd(qkv, c, o_b, flse, do_b, name="fox_pack_bwd", after=tok)
    dqp, dkp, dvp, dck, dcq = _fox_bwd(bqp, fkp, fvp, bdop, name="fox_bwd")
    dc = dcq[:, 0, :].T - dck.reshape(S, N_FOX_HEADS, HEAD_DIM)[:, :, 0]
    dc = jnp.pad(dc, ((0, 0), (0, F_PAD - N_FOX_HEADS)))
    dzf, db_fgt = _fox_cumsum_bwd(dc, zf, p["b_fgt"], name="fox_cumsum_bwd")

    douts = _dil_mix_bwd(do_a, dil_o, dil_l, name="dil_mix_bwd", after=tok)
    dqs, dks, dvs = [], [], []
    for g in range(N_DIL_GROUPS):
        dq, dk, dv = _dil_bwd(dil_qkv[g], douts[3 + g], douts[g], g, name=f"dil_bwd{g}")
        for parts, t in ((dqs, dq), (dks, dk), (dvs, dv)):
            parts.extend([t[0].astype(_CD), t[1].astype(_CD)])
    dqkv = jnp.concatenate(dqs + dks + dvs + [dqp, dkp, dvp], axis=1)

    gw_qkv = _mm(h, dqkv, name="gw_qkv", ta=True, out_dtype=_CD, tn=768)
    gw_g = _mm(h, dgl, name="gw_gate", ta=True, out_dtype=_CD)
    gw_f = _mm(h, dzf, name="gw_f", ta=True, out_dtype=_CD)
    tok = sink("in", dict(qkv=gw_qkv, f=gw_f, g=gw_g))
    dh = _mm(dqkv, w["qkv"], name="d_h_qkv", tb=True, tk=1920, tm=2048, after=tok)
    dh = _mm(dgl, w["g"], name="d_h_gate", tb=True, add=dh)
    dx, dg_mix = _mm(dzf, w["f"], name="d_h_f", tb=True, add=dh, tm=512, rms_bwd=(x, p["norm_mix_g"], dx1))

    gw = dict(qkv=gw_qkv, f=gw_f, g=gw_g, dil_out=gw_dil_out, fox_out=gw_fox_out, out=gw_out, ffn_in=gw_ffn_in,
              ffn_down=gw_ffn_down)
    small = dict(norm_mix_g=dg_mix, b_fgt=db_fgt, b_gate=db_gate, norm_ffn_g=dg_ffn, norm_final_g=dg_final)
    return loss, dx, gw, small


def _position():
    return lax.axis_index("x"), lax.axis_index("y"), lax.axis_index("c")


def _other_chips(x, y):
    return [(1 - x, y), (x, 1 - y), (1 - x, 1 - y)]


ROW_TILE = 16


def _row_chunks(rows, want=4):
    n = want
    while n > 1 and rows % (n * ROW_TILE):
        n //= 2
    return n


SEM_SPEC = pl.BlockSpec(memory_space=pltpu.SEMAPHORE)
ANY_SPEC = pl.BlockSpec(memory_space=pl.ANY)
DATAFLOW = pltpu.SideEffectType.DATAFLOW_SIDE_EFFECTING


def _in_hbm(a):
    return pltpu.with_memory_space_constraint(a, pltpu.HBM)


def _split_copy_start(srcs, land_shapes, copies, after, *, name):
    n, m = len(srcs), len(land_shapes)

    def body(*refs):
        src_refs, land_refs = refs[:n], refs[n:n + m]
        send_sems, recv_sems = refs[n + m + 1], refs[n + m + 2]
        token = refs[-1]
        x, y, c = _position()
        for k, (src, dst, peer) in enumerate(copies(x, y, c, src_refs, land_refs)):
            pltpu.make_async_remote_copy(src_ref=src, dst_ref=dst, send_sem=send_sems.at[k], recv_sem=recv_sems.at[k],
                                         device_id=peer, device_id_type=MESH).start()
        token[...] = jnp.zeros_like(token)

    lands = [lax.empty(s.shape, s.dtype) for s in land_shapes]
    count = len(copies(0, 0, 0, srcs, lands))
    out = _pcall(
        body, name=name,
        out_shape=(pltpu.SemaphoreType.DMA((count,)), pltpu.SemaphoreType.DMA((count,)),
                   *[pltpu.HBM(s.shape, s.dtype) for s in srcs], *[pltpu.HBM(s.shape, s.dtype) for s in land_shapes],
                   jax.ShapeDtypeStruct((8, 128), F32)),
        in_specs=[HBM_SPEC] * (n + m) + [ANY_SPEC],
        out_specs=(SEM_SPEC, SEM_SPEC, *[HBM_SPEC] * (n + m), pl.BlockSpec(memory_space=pltpu.VMEM)),
        input_output_aliases={k: 2 + k for k in range(n + m)},
        compiler_params=pltpu.CompilerParams(has_side_effects=DATAFLOW),
    )(*[_in_hbm(s) for s in srcs], *[_in_hbm(l) for l in lands], after)
    return out[0], out[1], list(out[2:2 + n]), list(out[2 + n:2 + n + m]), out[-1]


def _split_copy_wait(send_sems, recv_sems, srcs, lands, copies, after, *, name):
    n, m = len(srcs), len(lands)

    def body(*refs):
        src_refs, land_refs = refs[:n], refs[n:n + m]
        send, recv = refs[n + m], refs[n + m + 1]
        x, y, c = _position()
        for k, (src, dst, peer) in enumerate(copies(x, y, c, src_refs, land_refs)):
            cp = pltpu.make_async_remote_copy(src_ref=src, dst_ref=dst, send_sem=send.at[k], recv_sem=recv.at[k],
                                              device_id=peer, device_id_type=MESH)
            cp.wait_send()
            cp.wait_recv()

    afters = list(after) if isinstance(after, (list, tuple)) else [after]
    out = _pcall(
        body, name=name,
        out_shape=tuple(pltpu.HBM(s.shape, s.dtype) for s in list(srcs) + list(lands)),
        in_specs=[HBM_SPEC] * (n + m) + [SEM_SPEC, SEM_SPEC] + [ANY_SPEC] * len(afters),
        out_specs=tuple([HBM_SPEC] * (n + m)),
        input_output_aliases={k: k for k in range(n + m)},
        compiler_params=pltpu.CompilerParams(has_side_effects=DATAFLOW),
    )(*srcs, *lands, send_sems, recv_sems, *afters)
    return list(out[:n]), list(out[n:])


def _gather_copies(x, y, c, shard_refs, land_refs):
    out = []
    for s, l in zip(shard_refs, land_refs):
        half = s.shape[0] // 2
        nq = _row_chunks(half)
        for cx, cy in _other_chips(x, y):
            for q in range(nq):
                rows = pl.ds(c * half + q * (half // nq), half // nq)
                out.append((s.at[rows, :], l.at[2 * x + y, rows, :], (cx, cy, c)))
    return out


def _gather_whole_copies(x, y, c, shard_refs, land_refs):
    out = []
    for s, l in zip(shard_refs, land_refs):
        nq = _row_chunks(s.shape[0])
        for cx, cy in _other_chips(x, y):
            for q in range(nq):
                rows = pl.ds(q * (s.shape[0] // nq), s.shape[0] // nq)
                out.append((s.at[rows, :], l.at[2 * x + y, rows, :], (cx, cy, c)))
    return out


def _scatter_all_copies(x, y, c, block_refs, land_refs):
    out = []
    for g, l in zip(block_refs, land_refs):
        half = g.shape[1] // 2
        nq = _row_chunks(half)
        size = half // nq
        for q in range(nq):
            rows = pl.ds((1 - c) * half + q * size, size)
            out.append((g.at[2 * x + y, rows, :], l.at[0, pl.ds(q * size, size), :], (x, y, 1 - c)))
        for r, (cx, cy) in enumerate(_other_chips(x, y)):
            for j in range(2):
                h = c if j == 0 else 1 - c
                for q in range(nq):
                    rows = pl.ds(h * half + q * size, size)
                    out.append((g.at[2 * cx + cy, rows, :], l.at[1 + 2 * r + j, pl.ds(q * size, size), :], (cx, cy, h)))
    return out


def _forward_halves(lands, *, name):
    n = len(lands)

    def body(*refs):
        ins = refs[:n]
        send_sems, recv_sems = refs[2 * n:]
        x, y, c = _position()
        copies = []
        for w in range(n):
            half = ins[w].shape[1] // 2
            for r, (cx, cy) in enumerate(_other_chips(x, y)):
                blk = ins[w].at[2 * cx + cy, pl.ds(c * half, half), :]
                cp = pltpu.make_async_remote_copy(src_ref=blk, dst_ref=blk, send_sem=send_sems.at[w, r],
                                                  recv_sem=recv_sems.at[w, r], device_id=(x, y, 1 - c),
                                                  device_id_type=MESH)
                cp.start()
                copies.append(cp)
        for w in range(n):
            half = ins[w].shape[1] // 2
            for r, (cx, cy) in enumerate(_other_chips(x, y)):
                blk = ins[w].at[2 * cx + cy, pl.ds((1 - c) * half, half), :]
                pltpu.make_async_remote_copy(src_ref=blk, dst_ref=blk, send_sem=send_sems.at[w, r],
                                             recv_sem=recv_sems.at[w, r], device_id=(x, y, 1 - c),
                                             device_id_type=MESH).wait_recv()
        for cp in copies:
            cp.wait_send()

    return _pcall(
        body, name=name, in_specs=[HBM_SPEC] * n, out_specs=[HBM_SPEC] * n,
        out_shape=[jax.ShapeDtypeStruct(l.shape, l.dtype) for l in lands],
        input_output_aliases={k: k for k in range(n)},
        scratch_shapes=[pltpu.SemaphoreType.DMA((n, 3)), pltpu.SemaphoreType.DMA((n, 3))],
    )(*lands)


def _share_halves(halves):
    n = len(halves)

    def body(*refs):
        ins, outs = refs[:n], refs[n:2 * n]
        send_sems, recv_sems = refs[2 * n:]
        x, y, c = _position()
        copies = []
        for w in range(n):
            cp = pltpu.make_async_remote_copy(src_ref=ins[w], dst_ref=outs[w], send_sem=send_sems.at[w],
                                              recv_sem=recv_sems.at[w], device_id=(x, y, 1 - c), device_id_type=MESH)
            cp.start()
            copies.append(cp)
        for cp in copies:
            cp.wait()

    return _pcall(
        body, name="share_halves", in_specs=[HBM_SPEC] * n, out_specs=[HBM_SPEC] * n,
        out_shape=[jax.ShapeDtypeStruct(h.shape, h.dtype) for h in halves],
        scratch_shapes=[pltpu.SemaphoreType.DMA((n,)), pltpu.SemaphoreType.DMA((n,))],
    )(*halves)


def _sum_small(part):
    rows, width = part.shape

    def body(x_ref, out_ref, all_ref, send_sems, recv_sems):
        x, y, c = _position()
        me, sibling = (x, y, c), (x, y, 1 - c)
        chips = _other_chips(x, y)

        def block(px, py, pc):
            return all_ref.at[pl.ds((4 * px + 2 * py + pc) * rows, rows), :]

        def copy(k, blk, to, src=None):
            return pltpu.make_async_remote_copy(
                src_ref=block(*blk) if src is None else src, dst_ref=block(*blk), send_sem=send_sems.at[k],
                recv_sem=recv_sems.at[k], device_id=to, device_id_type=MESH)

        all_ref[pl.ds((4 * x + 2 * y + c) * rows, rows), :] = x_ref[...]
        first = [copy(0, me, sibling, src=x_ref)]
        first += [copy(1 + j, me, (*chip, c), src=x_ref) for j, chip in enumerate(chips)]
        for cp in first:
            cp.start()
        passed = [copy(4 + j, (*chip, c), sibling) for j, chip in enumerate(chips)]
        for j, chip in enumerate(chips):
            copy(1 + j, (*chip, c), me).wait_recv()
            passed[j].start()
        copy(0, sibling, me).wait_recv()
        for j, chip in enumerate(chips):
            copy(4 + j, (*chip, 1 - c), me).wait_recv()
        for cp in first + passed:
            cp.wait_send()
        total = all_ref[0:rows, :]
        for d in range(1, 8):
            total = total + all_ref[d * rows:(d + 1) * rows, :]
        out_ref[...] = total

    vm = pl.BlockSpec(memory_space=pltpu.VMEM)
    return _pcall(
        body, name="sum_small", in_specs=[vm], out_specs=vm, out_shape=jax.ShapeDtypeStruct((rows, width), F32),
        scratch_shapes=[pltpu.VMEM((8 * rows, width), F32), pltpu.SemaphoreType.DMA((7,)), pltpu.SemaphoreType.DMA((7,))],
    )(part)


def _row_tile(R, C, itemsize=4, budget=1 << 20):
    for t in (512, 256, 128, 64, 32, 16, 8):
        if R % t == 0 and t * C * itemsize <= budget:
            return t
    return R


def _add_all(g, recv, where, *, name):
    _, R, C = g.shape
    half = R // 2
    t = _row_tile(half, C)
    nb = half // t

    def body(w_ref, g_ref, r_ref, o_ref):
        total = g_ref[0].astype(F32)
        for k in range(7):
            total = total + r_ref[k].astype(F32)
        o_ref[...] = total

    grid_spec = pltpu.PrefetchScalarGridSpec(
        num_scalar_prefetch=1, grid=(nb,),
        in_specs=[pl.BlockSpec((1, t, C), lambda i, wr: (wr[0], wr[1] * nb + i, 0)),
                  pl.BlockSpec((7, t, C), lambda i, wr: (0, i, 0))],
        out_specs=pl.BlockSpec((t, C), lambda i, wr: (i, 0)))
    return _pcall(body, name=name, grid_spec=grid_spec, out_shape=jax.ShapeDtypeStruct((half, C), F32),
                  compiler_params=_params("parallel"))(where, g, recv)


def _adamw(w, g, m, v, *, name):
    R, C = w.shape
    t = _row_tile(R, C)
    c1 = 1.0 - ADAM_B1 ** ADAM_STEP
    c2 = 1.0 - ADAM_B2 ** ADAM_STEP

    def body(w_ref, g_ref, m_ref, v_ref, d_ref, nm_ref, nv_ref):
        gv = g_ref[...]
        mn = ADAM_B1 * m_ref[...] + (1.0 - ADAM_B1) * gv
        vn = ADAM_B2 * v_ref[...] + (1.0 - ADAM_B2) * (gv * gv)
        d_ref[...] = -ADAM_LR * ((mn / c1) / (jnp.sqrt(vn / c2) + ADAM_EPS) + ADAM_WD * w_ref[...])
        nm_ref[...] = mn
        nv_ref[...] = vn

    blk = pl.BlockSpec((t, C), lambda i: (i, 0))
    shp = jax.ShapeDtypeStruct((R, C), F32)
    return _pcall(body, name=name, grid=(R // t,), in_specs=[blk] * 4, out_specs=[blk] * 3, out_shape=[shp] * 3,
                  compiler_params=_params("parallel"))(w, g, m, v)


def _adamw_halves(w, mine, theirs, m, v, core, *, name):
    R, C = w.shape
    half = R // 2
    t = _row_tile(half, C)
    nbh = half // t
    c1 = 1.0 - ADAM_B1 ** ADAM_STEP
    c2 = 1.0 - ADAM_B2 ** ADAM_STEP

    def body(core_ref, w_ref, a_ref, b_ref, m_ref, v_ref, g_ref, d_ref, nm_ref, nv_ref):
        gv = jnp.where(pl.program_id(0) // nbh == core_ref[0], a_ref[...], b_ref[...])
        mn = ADAM_B1 * m_ref[...] + (1.0 - ADAM_B1) * gv
        vn = ADAM_B2 * v_ref[...] + (1.0 - ADAM_B2) * (gv * gv)
        g_ref[...] = gv
        d_ref[...] = -ADAM_LR * ((mn / c1) / (jnp.sqrt(vn / c2) + ADAM_EPS) + ADAM_WD * w_ref[...])
        nm_ref[...] = mn
        nv_ref[...] = vn

    blk = pl.BlockSpec((t, C), lambda i, cr: (i, 0))
    hblk = pl.BlockSpec((t, C), lambda i, cr: (i % nbh, 0))
    shp = jax.ShapeDtypeStruct((R, C), F32)
    grid_spec = pltpu.PrefetchScalarGridSpec(num_scalar_prefetch=1, grid=(2 * nbh,),
                                             in_specs=[blk, hblk, hblk, blk, blk], out_specs=[blk] * 4)
    return _pcall(body, name=name, grid_spec=grid_spec, out_shape=[shp] * 4,
                  compiler_params=_params("parallel"))(core, w, mine, theirs, m, v)


BIG = ("w_in", "w_dil_out", "w_fox_out", "w_out", "w_ffn_in", "w_ffn_down")
SMALL = ("norm_mix_g", "b_fgt", "b_gate", "norm_ffn_g", "norm_final_g")
ORDER = ("norm_mix_g", "w_in", "b_fgt", "b_gate", "w_dil_out", "w_fox_out", "w_out", "norm_ffn_g", "w_ffn_in",
         "w_ffn_down", "norm_final_g")
SMALL_ROWS = {"norm_mix_g": (0, 1), "b_gate": (1, 3), "norm_ffn_g": (3, 4), "norm_final_g": (4, 5), "b_fgt": (5, 6)}


def _columns_to_blocks(full, ncol):
    K = full.shape[0]
    return full.reshape(K, 4, ncol).transpose(1, 0, 2)


def _blocks_to_columns(blocks):
    n, K, ncol = blocks.shape
    return blocks.transpose(1, 0, 2).reshape(K, n * ncol)


def kernel(x, norm_mix_g, w_in, b_fgt, b_gate, w_dil_out, w_fox_out, w_out, norm_ffn_g, w_ffn_in, w_ffn_down, norm_final_g, loss_target, m_norm_mix_g, m_w_in, m_b_fgt, m_b_gate, m_w_dil_out, m_w_fox_out, m_w_out, m_norm_ffn_g, m_w_ffn_in, m_w_ffn_down, m_norm_final_g, v_norm_mix_g, v_w_in, v_b_fgt, v_b_gate, v_w_dil_out, v_w_fox_out, v_w_out, v_norm_ffn_g, v_w_ffn_in, v_w_ffn_down, v_norm_final_g):
    weights = dict(norm_mix_g=norm_mix_g, w_in=w_in, b_fgt=b_fgt, b_gate=b_gate, w_dil_out=w_dil_out,
                   w_fox_out=w_fox_out, w_out=w_out, norm_ffn_g=norm_ffn_g, w_ffn_in=w_ffn_in, w_ffn_down=w_ffn_down,
                   norm_final_g=norm_final_g)
    m_in = dict(norm_mix_g=m_norm_mix_g, w_in=m_w_in, b_fgt=m_b_fgt, b_gate=m_b_gate, w_dil_out=m_w_dil_out,
                w_fox_out=m_w_fox_out, w_out=m_w_out, norm_ffn_g=m_norm_ffn_g, w_ffn_in=m_w_ffn_in,
                w_ffn_down=m_w_ffn_down, norm_final_g=m_norm_final_g)
    v_in = dict(norm_mix_g=v_norm_mix_g, w_in=v_w_in, b_fgt=v_b_fgt, b_gate=v_b_gate, w_dil_out=v_w_dil_out,
                w_fox_out=v_w_fox_out, w_out=v_w_out, norm_ffn_g=v_norm_ffn_g, w_ffn_in=v_w_ffn_in,
                w_ffn_down=v_w_ffn_down, norm_final_g=v_norm_final_g)
    c = lax.axis_index("c")
    chip = 2 * lax.axis_index("x") + lax.axis_index("y")

    shards = {n: weights[n][0].astype(_CD) for n in BIG}
    in_shape = jax.ShapeDtypeStruct((4,) + shards["w_in"].shape, _CD)
    send_i, recv_i, in_src, in_land, token_in = _split_copy_start(
        [shards["w_in"]], [in_shape], _gather_copies, norm_mix_g, name="gather_in_start")
    late = BIG[1:]
    send_g, recv_g, late_src, late_land, token = _split_copy_start(
        [shards[n] for n in late], [jax.ShapeDtypeStruct((4,) + shards[n].shape, _CD) for n in late],
        _gather_whole_copies, token_in, name="gather_late_start")
    adam_in = [t[0] + token_in[0, 0] for t in (w_in, m_w_in, v_w_in)]
    p = dict(norm_mix_g=norm_mix_g, b_fgt=jnp.pad(b_fgt, ((0, 0), (0, F_PAD - N_FOX_HEADS))), b_gate=b_gate,
             norm_ffn_g=norm_ffn_g, norm_final_g=norm_final_g.reshape(1, D_MODEL))

    def first_weights(after):
        own, lands = _split_copy_wait(send_i, recv_i, in_src, in_land, _gather_copies, [after] + adam_in,
                                      name="gather_in_wait")
        (g_in,) = _forward_halves(lands, name="gather_in_forward")
        full_in = _blocks_to_columns(lax.dynamic_update_index_in_dim(g_in, own[0], chip, 0))
        o3 = QKV_COLS
        o4 = o3 + N_FOX_HEADS
        return dict(qkv=full_in[:, :o3], f=jnp.pad(full_in[:, o3:o4], ((0, 0), (0, F_PAD - N_FOX_HEADS))),
                    g=full_in[:, o4:])

    def late_weights(after):
        own, lands = _split_copy_wait(send_g, recv_g, late_src, late_land, _gather_whole_copies, after,
                                      name="gather_late_wait")
        g_dil, g_fox, g_out, g_ffn_in, g_ffn_down = [
            lax.dynamic_update_index_in_dim(l, s, chip, 0) for l, s in zip(lands, own)]
        return dict(dil_out=_blocks_to_columns(g_dil), fox_out=_blocks_to_columns(g_fox),
                    out=g_out.reshape(D_MODEL, D_MODEL), ffn_in=g_ffn_in,
                    ffn_down=g_ffn_down.reshape(D_FF, D_MODEL))

    def to_blocks(n, full):
        shape = weights[n].shape
        if full.ndim == 3:
            return full
        if n in ("w_out", "w_ffn_down"):
            return full.reshape(4, shape[1], shape[2])
        return _columns_to_blocks(full, shape[2])

    in_flight = {}

    def grad_sink(group, gw):
        if group == "in":
            named = {"w_in": jnp.concatenate([gw["qkv"], gw["f"][:, :N_FOX_HEADS], gw["g"]], axis=1)}
        else:
            named = {"w_" + k: v for k, v in gw.items()}
        srcs = [to_blocks(n, named[n]) for n in named]
        lands = [jax.ShapeDtypeStruct((7, s.shape[1] // 2, s.shape[2]), s.dtype) for s in srcs]
        started = _split_copy_start(srcs, lands, _scatter_all_copies, next(iter(gw.values())),
                                    name=f"scatter_{group}_start")
        in_flight[group] = (list(named), started)
        return started[-1]

    loss_part, grad_x, gw, small = _layer_step(x[0], loss_target[0], {}, p, late_weights, grad_sink,
                                               (token_in, token), first_weights)

    halves = {}
    where = jnp.stack([chip, c]).astype(jnp.int32)
    for group, (names, (send_s, recv_s, srcs, lands, _)) in in_flight.items():
        srcs, recv = _split_copy_wait(send_s, recv_s, srcs, lands, _scatter_all_copies, grad_x,
                                      name=f"scatter_{group}_wait")
        halves.update({n: _add_all(s, r, where, name=f"add_all_{n}") for n, s, r in zip(names, srcs, recv)})
    halves = [halves[n] for n in BIG]
    grad_halves = dict(zip(BIG, zip(halves, _share_halves(halves))))
    grads = {}

    packed = jnp.concatenate([
        small["norm_mix_g"], small["b_gate"].reshape(2, D_MODEL), small["norm_ffn_g"], small["norm_final_g"],
        jnp.pad(small["b_fgt"], ((0, 0), (0, D_MODEL - F_PAD))), jnp.pad(loss_part, ((0, 0), (0, D_MODEL - 1))),
        jnp.zeros((1, D_MODEL), F32)], axis=0)
    summed = _sum_small(packed)
    for n in SMALL:
        lo, hi = SMALL_ROWS[n]
        grads[n] = summed[lo:hi].reshape(1, -1)[:, :weights[n].size]
    loss = summed[6, 0]

    out_g, out_d, out_m, out_v = {}, {}, {}, {}
    core = jnp.reshape(c, (1,)).astype(jnp.int32)
    for n in ORDER:
        shape = weights[n].shape
        two_d = shape[1:] if len(shape) == 3 else (1, weights[n].size)
        wmv = adam_in if n == "w_in" else [t.reshape(two_d) for t in (weights[n], m_in[n], v_in[n])]
        if n in grad_halves:
            mine, theirs = grad_halves[n]
            g2, d2, m2, v2 = _adamw_halves(wmv[0], mine, theirs, wmv[1], wmv[2], core, name=f"adamw_{n}")
        else:
            g2 = grads[n].reshape(two_d)
            d2, m2, v2 = _adamw(wmv[0], g2, wmv[1], wmv[2], name=f"adamw_{n}")
        out_g[n], out_d[n], out_m[n], out_v[n] = (g2.reshape(shape), d2.reshape(shape), m2.reshape(shape),
                                                  v2.reshape(shape))
    return (loss, grad_x[None], *[out_g[n] for n in ORDER], *[out_d[n] for n in ORDER],
            *[out_m[n] for n in ORDER], *[out_v[n] for n in ORDER])
```

```python
import numpy as np
import jax
import jax.numpy as jnp
from jax import lax
from jax.experimental import pallas as pl
from jax.experimental.pallas import tpu as pltpu

F32 = jnp.float32
_CD = jnp.bfloat16

D_MODEL = 1024
HEAD_DIM = 64
DIL_PAIRS = ((128, 1), (512, 4), (2048, 16))
N_DIL_GROUPS = 3
DIL_HEADS = 4
DIL_W = 128
DIL_OUT = DIL_HEADS * HEAD_DIM
DIL_WIDTH = N_DIL_GROUPS * DIL_OUT
N_FOX_HEADS = 8
FOX_WIDTH = N_FOX_HEADS * HEAD_DIM
D_FF = 2816
QKV_COLS = 3 * DIL_WIDTH + 3 * FOX_WIDTH
F_PAD = 128
RMS_EPS = 1e-6
NEG_INF = -1e30
ATTN_SCALE = HEAD_DIM ** -0.5
ADAM_LR, ADAM_B1, ADAM_B2, ADAM_EPS, ADAM_WD, ADAM_STEP = 0.001, 0.9, 0.999, 1e-08, 0.01, 10

VMEM_LIMIT = 48 * 1024 * 1024
VMEM_LIMIT_RESIDENT = 56 * 1024 * 1024
LANES = 128
MESH = pl.DeviceIdType.MESH
HBM_SPEC = pl.BlockSpec(memory_space=pltpu.HBM)


def _pcall(body, after=None, **kw):
    if after is None:
        return pl.pallas_call(body, **kw)
    n_in = len(kw["in_specs"])
    kw["in_specs"] = list(kw["in_specs"]) + [pl.BlockSpec(memory_space=pl.ANY)]

    def tied(*refs):
        return body(*refs[:n_in], *refs[n_in + 1:])

    call = pl.pallas_call(tied, **kw)
    return lambda *args: call(*args, after)


def _params(*sem):
    return pltpu.CompilerParams(dimension_semantics=sem, vmem_limit_bytes=VMEM_LIMIT)


def _pick(dim, pref):
    t = (min(pref, dim) // 128) * 128
    while t >= 128:
        if dim % t == 0:
            return t
        t -= 128
    return dim


def _mm(a, b, *, name, ta=False, tb=False, out_dtype=F32, add=None, tm=1024, tn=512, tk=2048, after=None,
        b_blocks=False, out_blocks=None, a_halves=False, b_halves=False, rms_bwd=None):
    if a_halves:
        M, K = a.shape[1], 2 * a.shape[2]
    elif ta:
        K, M = a.shape
    else:
        M, K = a.shape
    if b_halves:
        b_rows, b_cols = b.shape[1], 2 * b.shape[2]
    else:
        b_rows, b_cols = (b.shape[1], b.shape[0] * b.shape[2]) if b_blocks else b.shape
    if tb:
        N, K2 = b_rows, b_cols
    else:
        K2, N = b_rows, b_cols
    assert K == K2, (a.shape, b.shape)
    shard = b.shape[2] if b_blocks else None
    tm = _pick(M, tm)
    tn = _pick(shard if (b_blocks and not tb) else (out_blocks or N), tn)
    tk = _pick(shard if (b_blocks and tb) else K, tk)
    nk = K // tk
    dn = (((0 if ta else 1,), (1 if tb else 0,)), ((), ()))
    has_add = add is not None
    assert not (has_add and out_blocks)
    has_norm = rms_bwd is not None
    if has_norm:
        tn = N
        assert not out_blocks and out_dtype == F32

    def body(*refs):
        a_ref, b_ref = refs[0], refs[1]
        rest = list(refs[2:])
        add_ref = rest.pop(0) if has_add else None
        x_ref, g_ref, dres_ref = (rest.pop(0), rest.pop(0), rest.pop(0)) if has_norm else (None, None, None)
        o_ref = rest.pop(0)
        dg_ref = rest.pop(0) if has_norm else None
        bv = b_ref[0] if b_blocks else b_ref[...]
        p = lax.dot_general(a_ref[...].astype(_CD), bv.astype(_CD), dn, preferred_element_type=F32)

        def finish(r):
            if has_add:
                r = r + add_ref[...]
            if has_norm:
                xv = x_ref[...]
                rs = lax.rsqrt(jnp.mean(xv * xv, axis=-1, keepdims=True) + RMS_EPS)
                xh = xv * rs
                dxh = r * g_ref[...]
                o_ref[...] = dres_ref[...] + rs * (dxh - xh * jnp.mean(dxh * xh, axis=-1, keepdims=True))
                part = jnp.sum(r * xh, axis=0, keepdims=True)
                first = pl.program_id(0) == 0

                @pl.when(first)
                def _():
                    dg_ref[...] = part

                @pl.when(jnp.logical_not(first))
                def _():
                    dg_ref[...] += part
            elif out_blocks:
                o_ref[0] = r.astype(out_dtype)
            else:
                o_ref[...] = r.astype(out_dtype)

        if nk == 1:
            finish(p)
        else:
            acc_ref = rest.pop(0)
            k = pl.program_id(2)

            @pl.when(k == 0)
            def _():
                acc_ref[...] = p

            @pl.when(k > 0)
            def _():
                acc_ref[...] += p

            @pl.when(k == nk - 1)
            def _():
                finish(acc_ref[...])

    if a_halves:
        ka = (K // 2) // tk
        a_spec = pl.BlockSpec((None, tm, tk), lambda i, j, k: (k // ka, i, k % ka))
    else:
        a_spec = pl.BlockSpec((tk, tm), lambda i, j, k: (k, i)) if ta else pl.BlockSpec((tm, tk), lambda i, j, k: (i, k))
    if b_halves:
        nb_ = (N // 2) // tn
        b_spec = pl.BlockSpec((None, tk, tn), lambda i, j, k: (j // nb_, k, j % nb_))
    elif b_blocks and tb:
        per = shard // tk
        b_spec = pl.BlockSpec((1, tn, tk), lambda i, j, k: (k // per, j, k % per))
    elif b_blocks:
        per = shard // tn
        b_spec = pl.BlockSpec((1, tk, tn), lambda i, j, k: (j // per, k, j % per))
    else:
        b_spec = pl.BlockSpec((tn, tk), lambda i, j, k: (j, k)) if tb else pl.BlockSpec((tk, tn), lambda i, j, k: (k, j))
    if out_blocks:
        oper = out_blocks // tn
        o_spec = pl.BlockSpec((1, tm, tn), lambda i, j, k: (j // oper, i, j % oper))
        out_shape = jax.ShapeDtypeStruct((N // out_blocks, M, out_blocks), out_dtype)
    else:
        o_spec = pl.BlockSpec((tm, tn), lambda i, j, k: (i, j))
        out_shape = jax.ShapeDtypeStruct((M, N), out_dtype)
    in_specs = [a_spec, b_spec] + ([o_spec] if has_add else [])
    args = (a, b) + ((add,) if has_add else ())
    out_specs, semantics = o_spec, ("parallel", "parallel", "arbitrary")
    if has_norm:
        vec = pl.BlockSpec((1, N), lambda i, j, k: (0, 0))
        in_specs += [o_spec, vec, o_spec]
        args += tuple(rms_bwd)
        out_specs, out_shape = [o_spec, vec], [out_shape, jax.ShapeDtypeStruct((1, N), F32)]
        semantics = ("arbitrary", "arbitrary", "arbitrary")
    return _pcall(
        body, after, name=name, grid=(M // tm, N // tn, nk), in_specs=in_specs, out_specs=out_specs,
        out_shape=out_shape,
        scratch_shapes=[pltpu.VMEM((tm, tn), F32)] if nk > 1 else [],
        compiler_params=_params(*semantics),
    )(*args)


def _rms_fwd(x, g, *, name, tm=512, after=None):
    S, D = x.shape

    def body(x_ref, g_ref, h_ref):
        xv = x_ref[...]
        r = lax.rsqrt(jnp.mean(xv * xv, axis=-1, keepdims=True) + RMS_EPS)
        h_ref[...] = ((xv * r) * g_ref[...]).astype(h_ref.dtype)

    row = pl.BlockSpec((tm, D), lambda i: (i, 0))
    return _pcall(body, after, name=name, grid=(S // tm,), in_specs=[row, pl.BlockSpec((1, D), lambda i: (0, 0))],
                  out_specs=row, out_shape=jax.ShapeDtypeStruct((S, D), _CD), compiler_params=_params("parallel"))(x, g)


def _ffn_down_loss(act, w_down, x1, g, tgt, *, name, tm=512):
    S, D = x1.shape
    F = act.shape[1]

    def body(a_ref, b_ref, x_ref, g_ref, t_ref, loss_ref, dx_ref, dg_ref):
        xv = x_ref[...] + jnp.dot(a_ref[...].astype(_CD), b_ref[...].astype(_CD), preferred_element_type=F32)
        gv = g_ref[...]
        r = lax.rsqrt(jnp.mean(xv * xv, axis=-1, keepdims=True) + RMS_EPS)
        xh = xv * r
        err = xh * gv - t_ref[...]
        lpart = 0.5 * jnp.sum(jnp.mean(err * err, axis=-1, keepdims=True), axis=0, keepdims=True)
        dy = err * (1.0 / D)
        dxh = dy * gv
        dx_ref[...] = r * (dxh - xh * jnp.mean(dxh * xh, axis=-1, keepdims=True))
        gpart = jnp.sum(dy * xh, axis=0, keepdims=True)

        @pl.when(pl.program_id(0) == 0)
        def _():
            loss_ref[...] = lpart
            dg_ref[...] = gpart

        @pl.when(pl.program_id(0) > 0)
        def _():
            loss_ref[...] += lpart
            dg_ref[...] += gpart

    row = pl.BlockSpec((tm, D), lambda i: (i, 0))
    vec = pl.BlockSpec((1, D), lambda i: (0, 0))
    one = pl.BlockSpec((1, 1), lambda i: (0, 0))
    return _pcall(body, name=name, grid=(S // tm,),
                  in_specs=[pl.BlockSpec((tm, F), lambda i: (i, 0)), pl.BlockSpec((F, D), lambda i: (0, 0)), row, vec, row],
                  out_specs=[one, row, vec],
                  out_shape=[jax.ShapeDtypeStruct((1, 1), F32), jax.ShapeDtypeStruct((S, D), F32),
                             jax.ShapeDtypeStruct((1, D), F32)],
                  compiler_params=_params("arbitrary"))(act, w_down, x1, g, tgt)


def _sigmoid(z):
    return 1.0 / (1.0 + jnp.exp(-z))


def _gated_mix_out(gl, bg, ya, yb, w_out, x, g, *, name, tm=512):
    S, D = ya.shape

    def body(za_ref, zb_ref, ba_ref, bb_ref, ya_ref, yb_ref, w_ref, x_ref, g_ref, m_ref, x1_ref, h_ref):
        ga = _sigmoid(za_ref[...].astype(F32) + ba_ref[...])
        gb = _sigmoid(zb_ref[...].astype(F32) + bb_ref[...])
        merged = (ga * ya_ref[...].astype(F32) + gb * yb_ref[...].astype(F32)).astype(m_ref.dtype)
        m_ref[...] = merged
        x1 = x_ref[...] + jnp.dot(merged, w_ref[...].astype(_CD), preferred_element_type=F32)
        x1_ref[...] = x1
        rs = lax.rsqrt(jnp.mean(x1 * x1, axis=-1, keepdims=True) + RMS_EPS)
        h_ref[...] = ((x1 * rs) * g_ref[...]).astype(h_ref.dtype)

    lo = pl.BlockSpec((tm, D), lambda i: (i, 0))
    hi = pl.BlockSpec((tm, D), lambda i: (i, 1))
    vlo = pl.BlockSpec((1, D), lambda i: (0, 0))
    vhi = pl.BlockSpec((1, D), lambda i: (0, 1))
    whole = pl.BlockSpec((D, D), lambda i: (0, 0))
    return _pcall(body, name=name, grid=(S // tm,), in_specs=[lo, hi, vlo, vhi, lo, lo, whole, lo, vlo],
                  out_specs=[lo, lo, lo],
                  out_shape=[jax.ShapeDtypeStruct((S, D), _CD), jax.ShapeDtypeStruct((S, D), F32),
                             jax.ShapeDtypeStruct((S, D), _CD)],
                  compiler_params=_params("parallel"))(gl, gl, bg, bg, ya, yb, w_out, x, g)


def _gate_bwd(dm, gl, bg, ya, yb, *, name, tm=256):
    S, D = ya.shape

    def body(dm_ref, za_ref, zb_ref, ba_ref, bb_ref, ya_ref, yb_ref, dya_ref, dyb_ref, dgl_ref, dbg_ref):
        dmv = dm_ref[...].astype(F32)
        ga = _sigmoid(za_ref[...].astype(F32) + ba_ref[...])
        gb = _sigmoid(zb_ref[...].astype(F32) + bb_ref[...])
        dya_ref[...] = (dmv * ga).astype(dya_ref.dtype)
        dyb_ref[...] = (dmv * gb).astype(dyb_ref.dtype)
        dza = dmv * ya_ref[...].astype(F32) * ga * (1.0 - ga)
        dzb = dmv * yb_ref[...].astype(F32) * gb * (1.0 - gb)
        dgl_ref[:, :D] = dza.astype(dgl_ref.dtype)
        dgl_ref[:, D:] = dzb.astype(dgl_ref.dtype)
        pa = jnp.sum(dza, axis=0, keepdims=True)
        pb = jnp.sum(dzb, axis=0, keepdims=True)

        @pl.when(pl.program_id(0) == 0)
        def _():
            dbg_ref[:, :D] = pa
            dbg_ref[:, D:] = pb

        @pl.when(pl.program_id(0) > 0)
        def _():
            dbg_ref[:, :D] += pa
            dbg_ref[:, D:] += pb

    lo = pl.BlockSpec((tm, D), lambda i: (i, 0))
    hi = pl.BlockSpec((tm, D), lambda i: (i, 1))
    vlo = pl.BlockSpec((1, D), lambda i: (0, 0))
    vhi = pl.BlockSpec((1, D), lambda i: (0, 1))
    wide = pl.BlockSpec((tm, 2 * D), lambda i: (i, 0))
    vwide = pl.BlockSpec((1, 2 * D), lambda i: (0, 0))
    return _pcall(body, name=name, grid=(S // tm,), in_specs=[lo, lo, hi, vlo, vhi, lo, lo],
                  out_specs=[lo, lo, wide, vwide],
                  out_shape=[jax.ShapeDtypeStruct((S, D), _CD), jax.ShapeDtypeStruct((S, D), _CD),
                             jax.ShapeDtypeStruct((S, 2 * D), _CD), jax.ShapeDtypeStruct((1, 2 * D), F32)],
                  compiler_params=_params("arbitrary"))(dm, gl, gl, bg, bg, ya, yb)


def _ffn_in_act(h2, w_blocks, *, name, tm=512):
    S, D = h2.shape
    _, _, C = w_blocks.shape

    def body(a_ref, bg_ref, bu_ref, g_ref, u_ref, o_ref):
        av = a_ref[...].astype(_CD)
        gv = jnp.dot(av, bg_ref[0].astype(_CD), preferred_element_type=F32)
        uv = jnp.dot(av, bu_ref[0].astype(_CD), preferred_element_type=F32)
        g_ref[...] = gv.astype(g_ref.dtype)
        u_ref[...] = uv.astype(u_ref.dtype)
        o_ref[...] = (gv * _sigmoid(gv) * uv).astype(o_ref.dtype)

    out = pl.BlockSpec((tm, C), lambda i, j: (i, j))
    shp = jax.ShapeDtypeStruct((S, 2 * C), _CD)
    return _pcall(body, name=name, grid=(S // tm, 2),
                  in_specs=[pl.BlockSpec((tm, D), lambda i, j: (i, 0)), pl.BlockSpec((1, D, C), lambda i, j: (j, 0, 0)),
                            pl.BlockSpec((1, D, C), lambda i, j: (2 + j, 0, 0))],
                  out_specs=[out, out, out], out_shape=[shp, shp, shp],
                  compiler_params=_params("parallel", "arbitrary"))(h2, w_blocks, w_blocks)


def _d_swiglu(dx, w_down, gate, up, *, name, tm=512, tn=1408):
    S, D = dx.shape
    F = w_down.shape[0]
    nt = (((1,), (1,)), ((), ()))

    def body(a_ref, b_ref, g_ref, u_ref, o_ref):
        dv = lax.dot_general(a_ref[...].astype(_CD), b_ref[...].astype(_CD), nt, preferred_element_type=F32)
        gv = g_ref[...].astype(F32)
        sg = _sigmoid(gv)
        o_ref[0] = (dv * u_ref[...].astype(F32) * (sg * (1.0 + gv * (1.0 - sg)))).astype(o_ref.dtype)
        o_ref[1] = (dv * (gv * sg)).astype(o_ref.dtype)

    tile = pl.BlockSpec((tm, tn), lambda i, j: (i, j))
    return _pcall(body, name=name, grid=(S // tm, F // tn),
                  in_specs=[pl.BlockSpec((tm, D), lambda i, j: (i, 0)), pl.BlockSpec((tn, D), lambda i, j: (j, 0)),
                            tile, tile],
                  out_specs=pl.BlockSpec((2, tm, tn), lambda i, j: (0, i, j)),
                  out_shape=jax.ShapeDtypeStruct((2, S, F), _CD),
                  compiler_params=_params("parallel", "arbitrary"))(dx, w_down, gate, up)


def _split3(x):
    hi = x.astype(jnp.bfloat16)
    r1 = x - hi.astype(F32)
    mid = r1.astype(jnp.bfloat16)
    lo = (r1 - mid.astype(F32)).astype(jnp.bfloat16)
    return hi, mid, lo


def _ones_dot_left(ones, x):
    return sum(jnp.dot(ones, p, preferred_element_type=F32) for p in _split3(x))


def _ones_dot_right(x, ones):
    return sum(jnp.dot(p, ones, preferred_element_type=F32) for p in _split3(x))


def _head_sum(x):
    n = x.shape[1]
    r = lax.broadcasted_iota(jnp.int32, (n, n), 0) // HEAD_DIM
    c = lax.broadcasted_iota(jnp.int32, (n, n), 1) // HEAD_DIM
    return _ones_dot_right(x, (r == c).astype(jnp.bfloat16))


def _log_sigmoid(z):
    e = jnp.exp(-jnp.abs(z))
    t = 1.0 + e
    log1p_e = jnp.where(t == 1.0, e, jnp.log(t) * (e / jnp.where(t == 1.0, 1.0, t - 1.0)))
    return jnp.minimum(z, 0.0) - log1p_e


def _fox_cumsum(zf, bf, *, name):
    S, W = zf.shape
    nb = S // 128

    def body(z_ref, b_ref, c_ref):
        tri = (lax.broadcasted_iota(jnp.int32, (128, 128), 0) >= lax.broadcasted_iota(jnp.int32, (128, 128), 1))
        tri = tri.astype(jnp.bfloat16)

        def step(i, carry):
            rows = pl.ds(pl.multiple_of(i * 128, 128), 128)
            lf = _log_sigmoid(z_ref[rows, :] + b_ref[...])
            cb = _ones_dot_left(tri, lf) + carry
            c_ref[rows, :] = cb
            return cb[127:128, :]

        lax.fori_loop(0, nb, step, jnp.zeros((1, W), F32))

    return _pcall(body, name=name, out_shape=jax.ShapeDtypeStruct((S, W), F32),
                  compiler_params=pltpu.CompilerParams(vmem_limit_bytes=VMEM_LIMIT))(zf, bf)


def _fox_cumsum_bwd(dc, zf, bf, *, name):
    S, W = zf.shape
    nb = S // 128

    def body(dc_ref, z_ref, b_ref, dz_ref, db_ref):
        tri = (lax.broadcasted_iota(jnp.int32, (128, 128), 0) <= lax.broadcasted_iota(jnp.int32, (128, 128), 1))
        tri = tri.astype(jnp.bfloat16)

        def step(k, carry):
            tail, acc = carry
            i = nb - 1 - k
            rows = pl.ds(pl.multiple_of(i * 128, 128), 128)
            dlf = _ones_dot_left(tri, dc_ref[rows, :]) + tail
            dz = dlf * _sigmoid(-(z_ref[rows, :] + b_ref[...]))
            dz_ref[rows, :] = dz
            return dlf[0:1, :], acc + jnp.sum(dz, axis=0, keepdims=True)

        _, acc = lax.fori_loop(0, nb, step, (jnp.zeros((1, W), F32), jnp.zeros((1, W), F32)))
        db_ref[...] = acc

    return _pcall(body, name=name,
                  out_shape=[jax.ShapeDtypeStruct((S, W), F32), jax.ShapeDtypeStruct((1, W), F32)],
                  compiler_params=pltpu.CompilerParams(vmem_limit_bytes=VMEM_LIMIT))(dc, zf, bf)


def _proj_dil(h, w_qkv, *, name, tm=1024):
    S, D = h.shape
    tn = DIL_WIDTH

    def body(a_ref, b_ref, *rest):
        outs, acc = rest[:N_DIL_GROUPS], rest[N_DIL_GROUPS]
        prod = jnp.dot(a_ref[...].astype(_CD), b_ref[...].astype(_CD), preferred_element_type=F32)
        for k in range(tn // LANES):
            acc[k] = prod[:, k * LANES:(k + 1) * LANES]
        for g, (_, d) in enumerate(DIL_PAIRS):
            for half in range(DIL_OUT // LANES):
                k = g * (DIL_OUT // LANES) + half
                cols = slice(half * LANES, (half + 1) * LANES)
                for r in range(d):
                    rows = pl.ds(r, tm // d, stride=d) if d > 1 else slice(None)
                    outs[g][0, r, :, cols] = acc[k, rows, :].astype(outs[g].dtype)

    out_specs = [pl.BlockSpec((1, d, tm // d, DIL_OUT), lambda i, j: (j, 0, i, 0)) for _, d in DIL_PAIRS]
    out_shape = [jax.ShapeDtypeStruct((3, d, S // d, DIL_OUT), _CD) for _, d in DIL_PAIRS]
    outs = _pcall(body, name=name, grid=(S // tm, 3),
                  in_specs=[pl.BlockSpec((tm, D), lambda i, j: (i, 0)), pl.BlockSpec((D, tn), lambda i, j: (0, j))],
                  out_specs=out_specs, out_shape=out_shape, scratch_shapes=[pltpu.VMEM((tn // LANES, tm, LANES), F32)],
                  compiler_params=_params("parallel", "arbitrary"))(h, w_qkv)
    return [o.reshape(3, S, DIL_OUT) for o in outs]


def _dil_start(block, S, dilation):
    sub = S // dilation
    u0 = block * DIL_W
    return (u0 % sub) * dilation + u0 // sub


def _dil_slopes(group):
    h = np.arange(1, N_DIL_GROUPS * DIL_HEADS + 1, dtype=np.float32)
    s = (np.float32(2.0) ** (np.float32(-8.0) * h / np.float32(N_DIL_GROUPS * DIL_HEADS))).astype(np.float32)
    return [float(v) for v in s.reshape(N_DIL_GROUPS, DIL_HEADS)[group]]


def _dil_tiles(i, n, blocks_per_seq):
    qi = lax.broadcasted_iota(jnp.int32, (DIL_W, 2 * DIL_W), 0)
    kj = lax.broadcasted_iota(jnp.int32, (DIL_W, 2 * DIL_W), 1)
    rel = qi + DIL_W - kj
    first = ((4 * n + i) % blocks_per_seq) == 0
    valid = jnp.logical_and(jnp.logical_and(rel >= 0, rel <= DIL_W), jnp.logical_or(kj >= DIL_W, jnp.logical_not(first)))
    return valid, rel.astype(F32)


def _dil_window(cur_ref, prev_ref, i, cols):
    if i > 0:
        return cur_ref[(i - 1) * DIL_W:(i + 1) * DIL_W, cols]
    return jnp.concatenate([prev_ref[:, cols], cur_ref[:DIL_W, cols]], axis=0)


CHUNK = 4 * DIL_W


def _dil_rows(block, S, dilation):
    start = _dil_start(block, S, dilation)
    return pl.ds(start, DIL_W, stride=dilation) if dilation > 1 else pl.ds(start, DIL_W)


def SPLIT(S):
    return (DIL_OUT // LANES, S, LANES)


def _dil_fwd(qkv, group, *, name):
    S = qkv.shape[1]
    dilation = DIL_PAIRS[group][1]
    bps = (S // dilation) // DIL_W
    slopes = _dil_slopes(group)
    nt = (((1,), (1,)), ((), ()))

    def body(q_ref, k_ref, v_ref, kp_ref, vp_ref, on_ref, ln_ref, o_ref, l_ref):
        n = pl.program_id(0)
        for i in range(4):
            valid, rel = _dil_tiles(i, n, bps)
            rows = slice(i * DIL_W, (i + 1) * DIL_W)
            for h in range(DIL_HEADS):
                cols = slice(h * HEAD_DIM, (h + 1) * HEAD_DIM)
                qh = q_ref[rows, cols]
                k2, v2 = _dil_window(k_ref, kp_ref, i, cols), _dil_window(v_ref, vp_ref, i, cols)
                s = lax.dot_general(qh, k2, nt, preferred_element_type=F32) * ATTN_SCALE - (slopes[h] * dilation) * rel
                s = jnp.where(valid, s, NEG_INF)
                m = jnp.max(s, axis=-1, keepdims=True)
                p = jnp.exp(s - m)
                den = jnp.sum(p, axis=-1, keepdims=True)
                acc = jnp.dot(p.astype(_CD), v2, preferred_element_type=F32)
                o_ref[rows, cols] = acc / den
                l_ref[rows, cols] = jnp.broadcast_to(m + jnp.log(den), (DIL_W, HEAD_DIM))
        for i in range(4):
            rows = slice(i * DIL_W, (i + 1) * DIL_W)
            nat = _dil_rows(4 * n + i, S, dilation)
            for half in range(DIL_OUT // LANES):
                cols = slice(half * LANES, (half + 1) * LANES)
                on_ref[half, nat, :] = o_ref[rows, cols]
                ln_ref[half, nat, :] = l_ref[rows, cols]

    def cur(which):
        return pl.BlockSpec((None, CHUNK, DIL_OUT), lambda n: (which, n, 0))

    def prev(which):
        return pl.BlockSpec((None, DIL_W, DIL_OUT), lambda n: (which, jnp.maximum(4 * n - 1, 0), 0))

    whole = pl.BlockSpec(SPLIT(S), lambda n: (0, 0, 0))
    return _pcall(body, name=name, grid=(S // CHUNK,), in_specs=[cur(0), cur(1), cur(2), prev(1), prev(2)],
                  out_specs=[whole, whole],
                  out_shape=[jax.ShapeDtypeStruct(SPLIT(S), F32), jax.ShapeDtypeStruct(SPLIT(S), F32)],
                  scratch_shapes=[pltpu.VMEM((CHUNK, DIL_OUT), F32), pltpu.VMEM((CHUNK, DIL_OUT), F32)],
                  compiler_params=_params("arbitrary"))(qkv, qkv, qkv, qkv, qkv)


STAT_OFFSET = HEAD_DIM // 2


def _dil_bwd(qkv, stats, do, group, *, name):
    S = qkv.shape[1]
    dilation = DIL_PAIRS[group][1]
    bps = (S // dilation) // DIL_W
    slopes = _dil_slopes(group)
    nchunk = S // CHUNK
    nt = (((1,), (1,)), ((), ()))
    tn = (((0,), (0,)), ((), ()))

    def body(q_ref, k_ref, v_ref, kp_ref, vp_ref, ln_ref, don_ref, dqn_ref, dkn_ref, dvn_ref,
             dk_s, dv_s, l_ref, do_ref, dq_ref):
        step = pl.program_id(0)
        n = nchunk - 1 - step
        for i in range(4):
            rows = slice(i * DIL_W, (i + 1) * DIL_W)
            nat = _dil_rows(4 * n + i, S, dilation)
            for half in range(DIL_OUT // LANES):
                cols = slice(half * LANES, (half + 1) * LANES)
                l_ref[rows, cols] = ln_ref[half, nat, :]
                do_ref[rows, cols] = don_ref[half, nat, :]

        @pl.when(step == 0)
        def _():
            dk_s[:, CHUNK:] = jnp.zeros((DIL_OUT, DIL_W), F32)
            dv_s[:, CHUNK:] = jnp.zeros((DIL_OUT, DIL_W), F32)

        dk_s[:, :CHUNK] = jnp.zeros((DIL_OUT, CHUNK), F32)
        dv_s[:, :CHUNK] = jnp.zeros((DIL_OUT, CHUNK), F32)
        for i in range(4):
            valid, rel = _dil_tiles(i, n, bps)
            rows = slice(i * DIL_W, (i + 1) * DIL_W)
            window = slice(i * DIL_W, (i + 2) * DIL_W)
            for h in range(DIL_HEADS):
                cols = slice(h * HEAD_DIM, (h + 1) * HEAD_DIM)
                qh = q_ref[rows, cols]
                k2, v2 = _dil_window(k_ref, kp_ref, i, cols), _dil_window(v_ref, vp_ref, i, cols)
                lh = l_ref[rows, h * HEAD_DIM:h * HEAD_DIM + 1]
                shift = l_ref[rows, h * HEAD_DIM + STAT_OFFSET:h * HEAD_DIM + STAT_OFFSET + 1]
                s = lax.dot_general(qh, k2, nt, preferred_element_type=F32) * ATTN_SCALE - (slopes[h] * dilation) * rel
                p = jnp.exp(jnp.where(valid, s, NEG_INF) - lh)
                dob = do_ref[rows, cols].astype(_CD)
                ds = p * (lax.dot_general(dob, v2, nt, preferred_element_type=F32) + shift)
                dsb = (ds * ATTN_SCALE).astype(_CD)
                dq_ref[rows, cols] = jnp.dot(dsb, k2, preferred_element_type=F32)
                dk_s[cols, window] += lax.dot_general(qh, dsb, tn, preferred_element_type=F32)
                dv_s[cols, window] += lax.dot_general(dob, p.astype(_CD), tn, preferred_element_type=F32)
        for i in range(4):
            rows = slice(i * DIL_W, (i + 1) * DIL_W)
            done = slice((i + 1) * DIL_W, (i + 2) * DIL_W)
            nat = _dil_rows(4 * n + i, S, dilation)
            dkb, dvb = dk_s[:, done].T, dv_s[:, done].T
            for half in range(DIL_OUT // LANES):
                cols = slice(half * LANES, (half + 1) * LANES)
                dqn_ref[half, nat, :] = dq_ref[rows, cols]
                dkn_ref[half, nat, :] = dkb[:, cols]
                dvn_ref[half, nat, :] = dvb[:, cols]
        dk_s[:, CHUNK:] = dk_s[:, :DIL_W]
        dv_s[:, CHUNK:] = dv_s[:, :DIL_W]

    def cur(which):
        return pl.BlockSpec((None, CHUNK, DIL_OUT), lambda s: (which, nchunk - 1 - s, 0))

    def prev(which):
        return pl.BlockSpec((None, DIL_W, DIL_OUT), lambda s: (which, jnp.maximum(4 * (nchunk - 1 - s) - 1, 0), 0))

    whole = pl.BlockSpec(SPLIT(S), lambda s: (0, 0, 0))
    shp = jax.ShapeDtypeStruct(SPLIT(S), F32)
    tile = pltpu.VMEM((CHUNK, DIL_OUT), F32)
    return _pcall(body, name=name, grid=(nchunk,),
                  in_specs=[cur(0), cur(1), cur(2), prev(1), prev(2), whole, whole],
                  out_specs=[whole, whole, whole], out_shape=[shp, shp, shp],
                  scratch_shapes=[pltpu.VMEM((DIL_OUT, CHUNK + DIL_W), F32), pltpu.VMEM((DIL_OUT, CHUNK + DIL_W), F32),
                                  tile, tile, tile],
                  compiler_params=pltpu.CompilerParams(dimension_semantics=("arbitrary",),
                                                       vmem_limit_bytes=VMEM_LIMIT_RESIDENT))(
        qkv, qkv, qkv, qkv, qkv, stats, do)


def _dil_mix_fwd(os_, ls_, *, name, tm=512):
    nh, S, _ = os_[0].shape

    def body(o0, o1, o2, l0, l1, l2, out_ref):
        for half in range(nh):
            ls = [l0[half], l1[half], l2[half]]
            m = jnp.maximum(jnp.maximum(ls[0], ls[1]), ls[2])
            es = [jnp.exp(l - m) for l in ls]
            den = es[0] + es[1] + es[2]
            mixed = (es[0] * o0[half] + es[1] * o1[half] + es[2] * o2[half]) / den
            out_ref[:, half * LANES:(half + 1) * LANES] = mixed.astype(out_ref.dtype)

    halves = pl.BlockSpec((nh, tm, LANES), lambda i: (0, i, 0))
    row = pl.BlockSpec((tm, nh * LANES), lambda i: (i, 0))
    return _pcall(body, name=name, grid=(S // tm,), in_specs=[halves] * 6, out_specs=row,
                  out_shape=jax.ShapeDtypeStruct((S, nh * LANES), _CD), compiler_params=_params("parallel"))(*os_, *ls_)


def _dil_mix_bwd(doa, os_, ls_, *, name, tm=512, after=None):
    nh, S, _ = os_[0].shape

    def body(d_ref, o0, o1, o2, l0, l1, l2, do0, do1, do2, st0, st1, st2):
        first = lax.broadcasted_iota(jnp.int32, (tm, LANES), 1) % HEAD_DIM < STAT_OFFSET
        for half in range(nh):
            dv = d_ref[:, half * LANES:(half + 1) * LANES]
            ls = [l0[half], l1[half], l2[half]]
            m = jnp.maximum(jnp.maximum(ls[0], ls[1]), ls[2])
            es = [jnp.exp(l - m) for l in ls]
            den = es[0] + es[1] + es[2]
            al = [e / den for e in es]
            da = [_head_sum(dv * o[half]) for o in (o0, o1, o2)]
            mean = al[0] * da[0] + al[1] * da[1] + al[2] * da[2]
            for a, l, do_ref, st_ref in zip(al, ls, (do0, do1, do2), (st0, st1, st2)):
                do_ref[half] = a * dv
                st_ref[half] = jnp.where(first, l, -a * mean)

    halves = pl.BlockSpec((nh, tm, LANES), lambda i: (0, i, 0))
    row = pl.BlockSpec((tm, nh * LANES), lambda i: (i, 0))
    shp = jax.ShapeDtypeStruct((nh, S, LANES), F32)
    return _pcall(body, after, name=name, grid=(S // tm,), in_specs=[row] + [halves] * 6, out_specs=[halves] * 6,
                  out_shape=[shp] * 6, compiler_params=_params("parallel"))(doa, *os_, *ls_)


FOX_T = 512


PACK = 2 * HEAD_DIM
HEAD_PAIRS = N_FOX_HEADS // 2
FOX_HPS = 8
Q_BLOCK0 = 0
K_BLOCK0 = FOX_WIDTH // PACK
V_BLOCK0 = 2 * FOX_WIDTH // PACK


def _pieces(x):
    hi = x.astype(jnp.bfloat16).astype(F32)
    r = x - hi
    mid = r.astype(jnp.bfloat16).astype(F32)
    lo = (r - mid).astype(jnp.bfloat16).astype(F32)
    return [hi, mid, lo]


def _extras(first, second, rows):
    lane = lax.broadcasted_iota(jnp.int32, (rows, HEAD_DIM), 1)
    out = jnp.zeros((rows, HEAD_DIM), F32)
    for base, triple in ((0, first), (3, second)):
        if all(isinstance(v, float) for v in triple) and len(set(triple)) == 1:
            if triple[0] != 0.0:
                out = jnp.where(jnp.logical_and(lane >= base, lane < base + 3), triple[0], out)
        else:
            for idx, val in enumerate(triple):
                out = jnp.where(lane == base + idx, val, out)
    return out


def _head_column(c, h):
    lane = lax.broadcasted_iota(jnp.int32, c.shape, 1)
    return jnp.sum(jnp.where(lane == h, c, 0.0), axis=1, keepdims=True)


ONES3 = [1.0, 1.0, 1.0]
ZEROS3 = [0.0, 0.0, 0.0]


def _fox_pack_fwd(qkv, c, *, name, tm=512):
    S = qkv.shape[0]

    def body(q_ref, k_ref, v_ref, c_ref, qo_ref, ko_ref, vo_ref):
        hp = pl.program_id(1)
        cv = c_ref[...]
        v_extras = jnp.where(lax.broadcasted_iota(jnp.int32, (tm, HEAD_DIM), 1) < 3, 1.0, 0.0).astype(vo_ref.dtype)
        for hh in range(2):
            ch = _pieces(_head_column(cv, 2 * hp + hh))
            src = slice(hh * HEAD_DIM, (hh + 1) * HEAD_DIM)
            lo = slice(hh * PACK, hh * PACK + HEAD_DIM)
            hi = slice(hh * PACK + HEAD_DIM, (hh + 1) * PACK)
            qo_ref[:, lo] = (q_ref[:, src].astype(F32) * ATTN_SCALE).astype(qo_ref.dtype)
            qo_ref[:, hi] = _extras(ch, ONES3, tm).astype(qo_ref.dtype)
            ko_ref[:, lo] = k_ref[:, src]
            ko_ref[:, hi] = _extras(ONES3, [-p for p in ch], tm).astype(ko_ref.dtype)
            vo_ref[:, lo] = v_ref[:, src]
            vo_ref[:, hi] = v_extras

    def src(block0):
        return pl.BlockSpec((tm, PACK), lambda i, hp: (i, block0 + hp))

    out = pl.BlockSpec((tm, 2 * PACK), lambda i, hp: (i, hp))
    shp = jax.ShapeDtypeStruct((S, N_FOX_HEADS * PACK), _CD)
    return _pcall(body, name=name, grid=(S // tm, HEAD_PAIRS),
                  in_specs=[src(Q_BLOCK0), src(K_BLOCK0), src(V_BLOCK0), pl.BlockSpec((tm, PACK), lambda i, hp: (i, 0))],
                  out_specs=[out, out, out], out_shape=[shp, shp, shp],
                  compiler_params=_params("parallel", "parallel"))(qkv, qkv, qkv, c)


def _fox_fwd(qp, kp, vp, *, name):
    S = qp.shape[0]
    nt = S // FOX_T
    nt_dims = (((1,), (1,)), ((), ()))
    tn_dims = (((0,), (0,)), ((), ()))

    def body(i_tab, j_tab, q_ref, k_ref, v_ref, o_ref, l_ref, m_s, acc_s):
        t = pl.program_id(1)
        i, j = i_tab[t], j_tab[t]

        @pl.when(j == 0)
        def _():
            m_s[...] = jnp.full((FOX_HPS, 1, FOX_T), NEG_INF, F32)
            acc_s[...] = jnp.zeros((FOX_HPS, PACK, FOX_T), F32)

        def tile(diagonal):
            for hh in range(FOX_HPS):
                cols = slice(hh * PACK, (hh + 1) * PACK)
                st = lax.dot_general(k_ref[:, cols], q_ref[:, cols], nt_dims, preferred_element_type=F32)
                if diagonal:
                    key = lax.broadcasted_iota(jnp.int32, (FOX_T, FOX_T), 0)
                    qry = lax.broadcasted_iota(jnp.int32, (FOX_T, FOX_T), 1)
                    st = jnp.where(key <= qry, st, NEG_INF)
                m_old = m_s[hh]
                m_new = jnp.maximum(m_old, jnp.max(st, axis=0, keepdims=True))
                pt = jnp.exp(st - m_new)
                acc_s[hh] = jnp.exp(m_old - m_new) * acc_s[hh] + lax.dot_general(
                    v_ref[:, cols], pt.astype(_CD), tn_dims, preferred_element_type=F32)
                m_s[hh] = m_new

        @pl.when(j < i)
        def _():
            tile(False)

        @pl.when(j == i)
        def _():
            tile(True)
            for hh in range(FOX_HPS):
                acc = acc_s[hh]
                den = acc[HEAD_DIM:HEAD_DIM + 1, :]
                cols = slice(hh * HEAD_DIM, (hh + 1) * HEAD_DIM)
                o_ref[:, cols] = (acc[:HEAD_DIM, :] / den).T
                l_ref[:, cols] = jnp.broadcast_to(m_s[hh] + jnp.log(den), (HEAD_DIM, FOX_T)).T

    pairs = [(i, j) for i in range(nt) for j in range(i + 1)]
    i_tab = jnp.asarray([p[0] for p in pairs], jnp.int32)
    j_tab = jnp.asarray([p[1] for p in pairs], jnp.int32)
    qs = pl.BlockSpec((FOX_T, FOX_HPS * PACK), lambda hp, t, it, jt: (it[t], hp))
    ks = pl.BlockSpec((FOX_T, FOX_HPS * PACK), lambda hp, t, it, jt: (jt[t], hp))
    os_ = pl.BlockSpec((FOX_T, FOX_HPS * HEAD_DIM), lambda hp, t, it, jt: (it[t], hp))
    shp = jax.ShapeDtypeStruct((S, FOX_WIDTH), F32)
    grid_spec = pltpu.PrefetchScalarGridSpec(
        num_scalar_prefetch=2, grid=(N_FOX_HEADS // FOX_HPS, len(pairs)), in_specs=[qs, ks, ks], out_specs=[os_, os_],
        scratch_shapes=[pltpu.VMEM((FOX_HPS, 1, FOX_T), F32), pltpu.VMEM((FOX_HPS, PACK, FOX_T), F32)])
    return _pcall(body, name=name, grid_spec=grid_spec, out_shape=[shp, shp],
                  compiler_params=_params("parallel", "arbitrary"))(i_tab, j_tab, qp, kp, vp)


def _fox_pack_bwd(qkv, c, o, lse, do, *, name, tm=512, after=None):
    S = qkv.shape[0]

    def body(q_ref, c_ref, o_ref, l_ref, do_ref, qo_ref, do_out_ref):
        hp = pl.program_id(1)
        cv = c_ref[...]
        for hh in range(2):
            src = slice(hh * HEAD_DIM, (hh + 1) * HEAD_DIM)
            lo = slice(hh * PACK, hh * PACK + HEAD_DIM)
            hi = slice(hh * PACK + HEAD_DIM, (hh + 1) * PACK)
            shift = _head_column(cv, 2 * hp + hh) - l_ref[:, hh * HEAD_DIM:hh * HEAD_DIM + 1]
            dov = do_ref[:, src]
            dsum = jnp.sum(dov * o_ref[:, src], axis=-1, keepdims=True)
            qo_ref[:, lo] = (q_ref[:, src].astype(F32) * ATTN_SCALE).astype(qo_ref.dtype)
            qo_ref[:, hi] = _extras(_pieces(shift), ONES3, tm).astype(qo_ref.dtype)
            do_out_ref[:, lo] = dov.astype(do_out_ref.dtype)
            do_out_ref[:, hi] = _extras(_pieces(-dsum), ZEROS3, tm).astype(do_out_ref.dtype)

    pair = pl.BlockSpec((tm, PACK), lambda i, hp: (i, hp))
    out = pl.BlockSpec((tm, 2 * PACK), lambda i, hp: (i, hp))
    shp = jax.ShapeDtypeStruct((S, N_FOX_HEADS * PACK), _CD)
    return _pcall(body, after, name=name, grid=(S // tm, HEAD_PAIRS),
                  in_specs=[pl.BlockSpec((tm, PACK), lambda i, hp: (i, Q_BLOCK0 + hp)),
                            pl.BlockSpec((tm, PACK), lambda i, hp: (i, 0)), pair, pair, pair],
                  out_specs=[out, out], out_shape=[shp, shp],
                  compiler_params=_params("parallel", "parallel"))(qkv, c, o, lse, do)


def _fox_bwd(qp, kp, vp, dop, *, name):
    S = qp.shape[0]
    nt = S // FOX_T
    nt_dims = (((1,), (1,)), ((), ()))
    tn_dims = (((0,), (0,)), ((), ()))

    def body(i_tab, j_tab, q_ref, k_ref, v_ref, do_ref, dq_ref, dk_ref, dv_ref, dc_ref, dr_ref,
             dq_s, dk_s, dv_s, dc_s, dr_s):
        t = pl.program_id(1)
        i, j = i_tab[t], j_tab[t]

        @pl.when(t == 0)
        def _():
            dq_s[...] = jnp.zeros((S, FOX_HPS * PACK), F32)
            dr_s[...] = jnp.zeros((FOX_HPS, 1, S), F32)

        @pl.when(i == j)
        def _():
            dk_s[...] = jnp.zeros((FOX_T, FOX_HPS * PACK), F32)
            dv_s[...] = jnp.zeros((FOX_T, FOX_HPS * PACK), F32)
            dc_s[...] = jnp.zeros((FOX_HPS, FOX_T, 1), F32)

        def tile(diagonal):
            rows = pl.ds(pl.multiple_of(i * FOX_T, FOX_T), FOX_T)
            for hh in range(FOX_HPS):
                cols = slice(hh * PACK, (hh + 1) * PACK)
                qv, kv, vv, dov = q_ref[:, cols], k_ref[:, cols], v_ref[:, cols], do_ref[:, cols]
                pt = jnp.exp(lax.dot_general(kv, qv, nt_dims, preferred_element_type=F32))
                if diagonal:
                    key = lax.broadcasted_iota(jnp.int32, (FOX_T, FOX_T), 0)
                    qry = lax.broadcasted_iota(jnp.int32, (FOX_T, FOX_T), 1)
                    pt = jnp.where(key <= qry, pt, 0.0)
                dst = pt * lax.dot_general(vv, dov, nt_dims, preferred_element_type=F32)
                dsb = dst.astype(_CD)
                dc_s[hh] += jnp.sum(dst, axis=1, keepdims=True)
                dr_s[hh, :, rows] += jnp.sum(dst, axis=0, keepdims=True)
                dv_s[:, cols] += jnp.dot(pt.astype(_CD), dov, preferred_element_type=F32)
                dk_s[:, cols] += jnp.dot(dsb, qv, preferred_element_type=F32)
                dq_s[rows, cols] += lax.dot_general(dsb, kv, tn_dims, preferred_element_type=F32)

        @pl.when(i > j)
        def _():
            tile(False)

        @pl.when(i == j)
        def _():
            tile(True)

        @pl.when(i == nt - 1)
        def _():
            for hh in range(FOX_HPS):
                src = slice(hh * PACK, hh * PACK + HEAD_DIM)
                dst_cols = slice(hh * HEAD_DIM, (hh + 1) * HEAD_DIM)
                dk_ref[:, dst_cols] = dk_s[:, src].astype(dk_ref.dtype)
                dv_ref[:, dst_cols] = dv_s[:, src].astype(dv_ref.dtype)
                dc_ref[:, dst_cols] = jnp.broadcast_to(dc_s[hh], (FOX_T, HEAD_DIM))

        @pl.when(t == len(pairs) - 1)
        def _():
            for hh in range(FOX_HPS):
                dq_ref[:, hh * HEAD_DIM:(hh + 1) * HEAD_DIM] = (
                    dq_s[:, hh * PACK:hh * PACK + HEAD_DIM] * ATTN_SCALE).astype(dq_ref.dtype)
            dr_ref[...] = dr_s[...]

    pairs = [(i, j) for j in range(nt) for i in range(j, nt)]
    i_tab = jnp.asarray([p[0] for p in pairs], jnp.int32)
    j_tab = jnp.asarray([p[1] for p in pairs], jnp.int32)
    wide, narrow = FOX_HPS * PACK, FOX_HPS * HEAD_DIM
    qs = pl.BlockSpec((FOX_T, wide), lambda hp, t, it, jt: (it[t], hp))
    ks = pl.BlockSpec((FOX_T, wide), lambda hp, t, it, jt: (jt[t], hp))
    whole = pl.BlockSpec((S, narrow), lambda hp, t, it, jt: (0, hp))
    cs = pl.BlockSpec((FOX_T, narrow), lambda hp, t, it, jt: (jt[t], hp))
    rs = pl.BlockSpec((FOX_HPS, 1, S), lambda hp, t, it, jt: (hp, 0, 0))
    shp = jax.ShapeDtypeStruct((S, FOX_WIDTH), _CD)
    grid_spec = pltpu.PrefetchScalarGridSpec(
        num_scalar_prefetch=2, grid=(N_FOX_HEADS // FOX_HPS, len(pairs)), in_specs=[qs, ks, ks, qs],
        out_specs=[whole, cs, cs, cs, rs],
        scratch_shapes=[pltpu.VMEM((S, wide), F32), pltpu.VMEM((FOX_T, wide), F32),
                        pltpu.VMEM((FOX_T, wide), F32), pltpu.VMEM((FOX_HPS, FOX_T, 1), F32),
                        pltpu.VMEM((FOX_HPS, 1, S), F32)])
    return _pcall(body, name=name, grid_spec=grid_spec,
                  out_shape=[shp, shp, shp, jax.ShapeDtypeStruct((S, FOX_WIDTH), F32),
                             jax.ShapeDtypeStruct((N_FOX_HEADS, 1, S), F32)],
                  compiler_params=_params("parallel", "arbitrary"))(i_tab, j_tab, qp, kp, vp, dop)


def _layer_step(x, tgt, w, p, late_weights=None, grad_sink=None, after=None, first_weights=None):
    S = x.shape[0]
    after_norm, after_proj = after if after is not None else (None, None)
    h = _rms_fwd(x, p["norm_mix_g"], name="rms_mix", after=after_norm)
    if first_weights is not None:
        w = {**w, **first_weights(h)}
    qkv = _mm(h, w["qkv"][:, 3 * DIL_WIDTH:], name="proj_fox", out_dtype=_CD, tn=768, tm=2048, after=after_proj)
    dil_qkv = _proj_dil(h, w["qkv"], name="proj_dil")
    zf = _mm(h, w["f"], name="proj_f")
    gl = _mm(h, w["g"], name="proj_gate", tn=1024, out_dtype=_CD)

    dil_o, dil_l = [], []
    for g in range(N_DIL_GROUPS):
        og, lg = _dil_fwd(dil_qkv[g], g, name=f"dil_fwd{g}")
        dil_o.append(og), dil_l.append(lg)
    o_a = _dil_mix_fwd(dil_o, dil_l, name="dil_mix")

    c = _fox_cumsum(zf, p["b_fgt"], name="fox_cumsum")
    fqp, fkp, fvp = _fox_pack_fwd(qkv, c, name="fox_pack")
    o_b, flse = _fox_fwd(fqp, fkp, fvp, name="fox_fwd")

    if late_weights is not None:
        w = {**w, **late_weights(o_b)}
    y_a = _mm(o_a, w["dil_out"], name="y_a", tn=1024, out_dtype=_CD)
    y_b = _mm(o_b, w["fox_out"], name="y_b", tn=1024, out_dtype=_CD)
    merged, x1, h2 = _gated_mix_out(gl, p["b_gate"], y_a, y_b, w["out"], x, p["norm_ffn_g"], name="mix_out")
    gate, up, act = _ffn_in_act(h2, w["ffn_in"], name="ffn_in")
    loss, dx2, dg_final = _ffn_down_loss(act, w["ffn_down"], x1, p["norm_final_g"], tgt, name="ffn_down_loss")

    gw_ffn_down = _mm(act, dx2, name="gw_ffn_down", ta=True, out_dtype=_CD, tm=1408)
    dgu = _d_swiglu(dx2, w["ffn_down"], gate, up, name="d_swiglu")
    gw_ffn_in = _mm(h2, dgu, name="gw_ffn_in", ta=True, out_dtype=_CD, tn=1408, out_blocks=1408, b_halves=True)
    sink = grad_sink if grad_sink is not None else (lambda group, grads: None)
    tok = sink("ffn", dict(ffn_in=gw_ffn_in, ffn_down=gw_ffn_down))
    dx1, dg_ffn = _mm(dgu, w["ffn_in"], name="d_h2", tb=True, tk=1408, b_blocks=True, tm=1024, a_halves=True,
                      rms_bwd=(x1, p["norm_ffn_g"], dx2), after=tok)

    dmerged = _mm(dx1, w["out"], name="d_merged", tb=True, out_dtype=_CD)
    gw_out = _mm(merged, dx1, name="gw_out", ta=True, out_dtype=_CD)
    dy_a, dy_b, dgl, db_gate = _gate_bwd(dmerged, gl, p["b_gate"], y_a, y_b, name="gate_bwd")
    do_a = _mm(dy_a, w["dil_out"], name="d_o_a", tb=True)
    gw_dil_out = _mm(o_a, dy_a, name="gw_dil_out", ta=True, out_dtype=_CD, tn=1024)
    do_b = _mm(dy_b, w["fox_out"], name="d_o_b", tb=True)
    gw_fox_out = _mm(o_b, dy_b, name="gw_fox_out", ta=True, out_dtype=_CD, tn=1024)
    tok = sink("mix", dict(dil_out=gw_dil_out, fox_out=gw_fox_out, out=gw_out))

    bqp, bdop = _fox_pack_bwd(qkv, c, o_b, flse, do_b, name="fox_pack_bwd", after=tok)
    dqp, dkp, dvp, dck, dcq = _fox_bwd(bqp, fkp, fvp, bdop, name="fox_bwd")
    dc = dcq[:, 0, :].T - dck.reshape(S, N_FOX_HEADS, HEAD_DIM)[:, :, 0]
    dc = jnp.pad(dc, ((0, 0), (0, F_PAD - N_FOX_HEADS)))
    dzf, db_fgt = _fox_cumsum_bwd(dc, zf, p["b_fgt"], name="fox_cumsum_bwd")

    douts = _dil_mix_bwd(do_a, dil_o, dil_l, name="dil_mix_bwd", after=tok)
    dqs, dks, dvs = [], [], []
    for g in range(N_DIL_GROUPS):
        dq, dk, dv = _dil_bwd(dil_qkv[g], douts[3 + g], douts[g], g, name=f"dil_bwd{g}")
        for parts, t in ((dqs, dq), (dks, dk), (dvs, dv)):
            parts.extend([t[0].astype(_CD), t[1].astype(_CD)])
    dqkv = jnp.concatenate(dqs + dks + dvs + [dqp, dkp, dvp], axis=1)

    gw_qkv = _mm(h, dqkv, name="gw_qkv", ta=True, out_dtype=_CD, tn=768)
    gw_g = _mm(h, dgl, name="gw_gate", ta=True, out_dtype=_CD)
    gw_f = _mm(h, dzf, name="gw_f", ta=True, out_dtype=_CD)
    tok = sink("in", dict(qkv=gw_qkv, f=gw_f, g=gw_g))
    dh = _mm(dqkv, w["qkv"], name="d_h_qkv", tb=True, tk=1920, tm=2048, after=tok)
    dh = _mm(dgl, w["g"], name="d_h_gate", tb=True, add=dh)
    dx, dg_mix = _mm(dzf, w["f"], name="d_h_f", tb=True, add=dh, tm=512, rms_bwd=(x, p["norm_mix_g"], dx1))

    gw = dict(qkv=gw_qkv, f=gw_f, g=gw_g, dil_out=gw_dil_out, fox_out=gw_fox_out, out=gw_out, ffn_in=gw_ffn_in,
              ffn_down=gw_ffn_down)
    small = dict(norm_mix_g=dg_mix, b_fgt=db_fgt, b_gate=db_gate, norm_ffn_g=dg_ffn, norm_final_g=dg_final)
    return loss, dx, gw, small


def _position():
    return lax.axis_index("x"), lax.axis_index("y"), lax.axis_index("c")


def _other_chips(x, y):
    return [(1 - x, y), (x, 1 - y), (1 - x, 1 - y)]


ROW_TILE = 16


def _row_chunks(rows, want=4):
    n = want
    while n > 1 and rows % (n * ROW_TILE):
        n //= 2
    return n


SEM_SPEC = pl.BlockSpec(memory_space=pltpu.SEMAPHORE)
ANY_SPEC = pl.BlockSpec(memory_space=pl.ANY)
DATAFLOW = pltpu.SideEffectType.DATAFLOW_SIDE_EFFECTING


def _in_hbm(a):
    return pltpu.with_memory_space_constraint(a, pltpu.HBM)


def _split_copy_start(srcs, land_shapes, copies, after, *, name):
    n, m = len(srcs), len(land_shapes)

    def body(*refs):
        src_refs, land_refs = refs[:n], refs[n:n + m]
        send_sems, recv_sems = refs[n + m + 1], refs[n + m + 2]
        token = refs[-1]
        x, y, c = _position()
        for k, (src, dst, peer) in enumerate(copies(x, y, c, src_refs, land_refs)):
            pltpu.make_async_remote_copy(src_ref=src, dst_ref=dst, send_sem=send_sems.at[k], recv_sem=recv_sems.at[k],
                                         device_id=peer, device_id_type=MESH).start()
        token[...] = jnp.zeros_like(token)

    lands = [lax.empty(s.shape, s.dtype) for s in land_shapes]
    count = len(copies(0, 0, 0, srcs, lands))
    out = _pcall(
        body, name=name,
        out_shape=(pltpu.SemaphoreType.DMA((count,)), pltpu.SemaphoreType.DMA((count,)),
                   *[pltpu.HBM(s.shape, s.dtype) for s in srcs], *[pltpu.HBM(s.shape, s.dtype) for s in land_shapes],
                   jax.ShapeDtypeStruct((8, 128), F32)),
        in_specs=[HBM_SPEC] * (n + m) + [ANY_SPEC],
        out_specs=(SEM_SPEC, SEM_SPEC, *[HBM_SPEC] * (n + m), pl.BlockSpec(memory_space=pltpu.VMEM)),
        input_output_aliases={k: 2 + k for k in range(n + m)},
        compiler_params=pltpu.CompilerParams(has_side_effects=DATAFLOW),
    )(*[_in_hbm(s) for s in srcs], *[_in_hbm(l) for l in lands], after)
    return out[0], out[1], list(out[2:2 + n]), list(out[2 + n:2 + n + m]), out[-1]


def _split_copy_wait(send_sems, recv_sems, srcs, lands, copies, after, *, name):
    n, m = len(srcs), len(lands)

    def body(*refs):
        src_refs, land_refs = refs[:n], refs[n:n + m]
        send, recv = refs[n + m], refs[n + m + 1]
        x, y, c = _position()
        for k, (src, dst, peer) in enumerate(copies(x, y, c, src_refs, land_refs)):
            cp = pltpu.make_async_remote_copy(src_ref=src, dst_ref=dst, send_sem=send.at[k], recv_sem=recv.at[k],
                                              device_id=peer, device_id_type=MESH)
            cp.wait_send()
            cp.wait_recv()

    afters = list(after) if isinstance(after, (list, tuple)) else [after]
    out = _pcall(
        body, name=name,
        out_shape=tuple(pltpu.HBM(s.shape, s.dtype) for s in list(srcs) + list(lands)),
        in_specs=[HBM_SPEC] * (n + m) + [SEM_SPEC, SEM_SPEC] + [ANY_SPEC] * len(afters),
        out_specs=tuple([HBM_SPEC] * (n + m)),
        input_output_aliases={k: k for k in range(n + m)},
        compiler_params=pltpu.CompilerParams(has_side_effects=DATAFLOW),
    )(*srcs, *lands, send_sems, recv_sems, *afters)
    return list(out[:n]), list(out[n:])


def _gather_copies(x, y, c, shard_refs, land_refs):
    out = []
    for s, l in zip(shard_refs, land_refs):
        half = s.shape[0] // 2
        nq = _row_chunks(half)
        for cx, cy in _other_chips(x, y):
            for q in range(nq):
                rows = pl.ds(c * half + q * (half // nq), half // nq)
                out.append((s.at[rows, :], l.at[2 * x + y, rows, :], (cx, cy, c)))
    return out


def _gather_whole_copies(x, y, c, shard_refs, land_refs):
    out = []
    for s, l in zip(shard_refs, land_refs):
        nq = _row_chunks(s.shape[0])
        for cx, cy in _other_chips(x, y):
            for q in range(nq):
                rows = pl.ds(q * (s.shape[0] // nq), s.shape[0] // nq)
                out.append((s.at[rows, :], l.at[2 * x + y, rows, :], (cx, cy, c)))
    return out


def _scatter_all_copies(x, y, c, block_refs, land_refs):
    out = []
    for g, l in zip(block_refs, land_refs):
        half = g.shape[1] // 2
        nq = _row_chunks(half)
        size = half // nq
        for q in range(nq):
            rows = pl.ds((1 - c) * half + q * size, size)
            out.append((g.at[2 * x + y, rows, :], l.at[0, pl.ds(q * size, size), :], (x, y, 1 - c)))
        for r, (cx, cy) in enumerate(_other_chips(x, y)):
            for j in range(2):
                h = c if j == 0 else 1 - c
                for q in range(nq):
                    rows = pl.ds(h * half + q * size, size)
                    out.append((g.at[2 * cx + cy, rows, :], l.at[1 + 2 * r + j, pl.ds(q * size, size), :], (cx, cy, h)))
    return out


def _forward_halves(lands, *, name):
    n = len(lands)

    def body(*refs):
        ins = refs[:n]
        send_sems, recv_sems = refs[2 * n:]
        x, y, c = _position()
        copies = []
        for w in range(n):
            half = ins[w].shape[1] // 2
            for r, (cx, cy) in enumerate(_other_chips(x, y)):
                blk = ins[w].at[2 * cx + cy, pl.ds(c * half, half), :]
                cp = pltpu.make_async_remote_copy(src_ref=blk, dst_ref=blk, send_sem=send_sems.at[w, r],
                                                  recv_sem=recv_sems.at[w, r], device_id=(x, y, 1 - c),
                                                  device_id_type=MESH)
                cp.start()
                copies.append(cp)
        for w in range(n):
            half = ins[w].shape[1] // 2
            for r, (cx, cy) in enumerate(_other_chips(x, y)):
                blk = ins[w].at[2 * cx + cy, pl.ds((1 - c) * half, half), :]
                pltpu.make_async_remote_copy(src_ref=blk, dst_ref=blk, send_sem=send_sems.at[w, r],
                                             recv_sem=recv_sems.at[w, r], device_id=(x, y, 1 - c),
                                             device_id_type=MESH).wait_recv()
        for cp in copies:
            cp.wait_send()

    return _pcall(
        body, name=name, in_specs=[HBM_SPEC] * n, out_specs=[HBM_SPEC] * n,
        out_shape=[jax.ShapeDtypeStruct(l.shape, l.dtype) for l in lands],
        input_output_aliases={k: k for k in range(n)},
        scratch_shapes=[pltpu.SemaphoreType.DMA((n, 3)), pltpu.SemaphoreType.DMA((n, 3))],
    )(*lands)


def _share_halves(halves):
    n = len(halves)

    def body(*refs):
        ins, outs = refs[:n], refs[n:2 * n]
        send_sems, recv_sems = refs[2 * n:]
        x, y, c = _position()
        copies = []
        for w in range(n):
            cp = pltpu.make_async_remote_copy(src_ref=ins[w], dst_ref=outs[w], send_sem=send_sems.at[w],
                                              recv_sem=recv_sems.at[w], device_id=(x, y, 1 - c), device_id_type=MESH)
            cp.start()
            copies.append(cp)
        for cp in copies:
            cp.wait()

    return _pcall(
        body, name="share_halves", in_specs=[HBM_SPEC] * n, out_specs=[HBM_SPEC] * n,
        out_shape=[jax.ShapeDtypeStruct(h.shape, h.dtype) for h in halves],
        scratch_shapes=[pltpu.SemaphoreType.DMA((n,)), pltpu.SemaphoreType.DMA((n,))],
    )(*halves)


def _sum_small(part):
    rows, width = part.shape

    def body(x_ref, out_ref, all_ref, send_sems, recv_sems):
        x, y, c = _position()
        me, sibling = (x, y, c), (x, y, 1 - c)
        chips = _other_chips(x, y)

        def block(px, py, pc):
            return all_ref.at[pl.ds((4 * px + 2 * py + pc) * rows, rows), :]

        def copy(k, blk, to, src=None):
            return pltpu.make_async_remote_copy(
                src_ref=block(*blk) if src is None else src, dst_ref=block(*blk), send_sem=send_sems.at[k],
                recv_sem=recv_sems.at[k], device_id=to, device_id_type=MESH)

        all_ref[pl.ds((4 * x + 2 * y + c) * rows, rows), :] = x_ref[...]
        first = [copy(0, me, sibling, src=x_ref)]
        first += [copy(1 + j, me, (*chip, c), src=x_ref) for j, chip in enumerate(chips)]
        for cp in first:
            cp.start()
        passed = [copy(4 + j, (*chip, c), sibling) for j, chip in enumerate(chips)]
        for j, chip in enumerate(chips):
            copy(1 + j, (*chip, c), me).wait_recv()
            passed[j].start()
        copy(0, sibling, me).wait_recv()
        for j, chip in enumerate(chips):
            copy(4 + j, (*chip, 1 - c), me).wait_recv()
        for cp in first + passed:
            cp.wait_send()
        total = all_ref[0:rows, :]
        for d in range(1, 8):
            total = total + all_ref[d * rows:(d + 1) * rows, :]
        out_ref[...] = total

    vm = pl.BlockSpec(memory_space=pltpu.VMEM)
    return _pcall(
        body, name="sum_small", in_specs=[vm], out_specs=vm, out_shape=jax.ShapeDtypeStruct((rows, width), F32),
        scratch_shapes=[pltpu.VMEM((8 * rows, width), F32), pltpu.SemaphoreType.DMA((7,)), pltpu.SemaphoreType.DMA((7,))],
    )(part)


def _row_tile(R, C, itemsize=4, budget=1 << 20):
    for t in (512, 256, 128, 64, 32, 16, 8):
        if R % t == 0 and t * C * itemsize <= budget:
            return t
    return R


def _add_all(g, recv, where, *, name):
    _, R, C = g.shape
    half = R // 2
    t = _row_tile(half, C)
    nb = half // t

    def body(w_ref, g_ref, r_ref, o_ref):
        total = g_ref[0].astype(F32)
        for k in range(7):
            total = total + r_ref[k].astype(F32)
        o_ref[...] = total

    grid_spec = pltpu.PrefetchScalarGridSpec(
        num_scalar_prefetch=1, grid=(nb,),
        in_specs=[pl.BlockSpec((1, t, C), lambda i, wr: (wr[0], wr[1] * nb + i, 0)),
                  pl.BlockSpec((7, t, C), lambda i, wr: (0, i, 0))],
        out_specs=pl.BlockSpec((t, C), lambda i, wr: (i, 0)))
    return _pcall(body, name=name, grid_spec=grid_spec, out_shape=jax.ShapeDtypeStruct((half, C), F32),
                  compiler_params=_params("parallel"))(where, g, recv)


def _adamw(w, g, m, v, *, name):
    R, C = w.shape
    t = _row_tile(R, C)
    c1 = 1.0 - ADAM_B1 ** ADAM_STEP
    c2 = 1.0 - ADAM_B2 ** ADAM_STEP

    def body(w_ref, g_ref, m_ref, v_ref, d_ref, nm_ref, nv_ref):
        gv = g_ref[...]
        mn = ADAM_B1 * m_ref[...] + (1.0 - ADAM_B1) * gv
        vn = ADAM_B2 * v_ref[...] + (1.0 - ADAM_B2) * (gv * gv)
        d_ref[...] = -ADAM_LR * ((mn / c1) / (jnp.sqrt(vn / c2) + ADAM_EPS) + ADAM_WD * w_ref[...])
        nm_ref[...] = mn
        nv_ref[...] = vn

    blk = pl.BlockSpec((t, C), lambda i: (i, 0))
    shp = jax.ShapeDtypeStruct((R, C), F32)
    return _pcall(body, name=name, grid=(R // t,), in_specs=[blk] * 4, out_specs=[blk] * 3, out_shape=[shp] * 3,
                  compiler_params=_params("parallel"))(w, g, m, v)


def _adamw_halves(w, mine, theirs, m, v, core, *, name):
    R, C = w.shape
    half = R // 2
    t = _row_tile(half, C)
    nbh = half // t
    c1 = 1.0 - ADAM_B1 ** ADAM_STEP
    c2 = 1.0 - ADAM_B2 ** ADAM_STEP

    def body(core_ref, w_ref, a_ref, b_ref, m_ref, v_ref, g_ref, d_ref, nm_ref, nv_ref):
        gv = jnp.where(pl.program_id(0) // nbh == core_ref[0], a_ref[...], b_ref[...])
        mn = ADAM_B1 * m_ref[...] + (1.0 - ADAM_B1) * gv
        vn = ADAM_B2 * v_ref[...] + (1.0 - ADAM_B2) * (gv * gv)
        g_ref[...] = gv
        d_ref[...] = -ADAM_LR * ((mn / c1) / (jnp.sqrt(vn / c2) + ADAM_EPS) + ADAM_WD * w_ref[...])
        nm_ref[...] = mn
        nv_ref[...] = vn

    blk = pl.BlockSpec((t, C), lambda i, cr: (i, 0))
    hblk = pl.BlockSpec((t, C), lambda i, cr: (i % nbh, 0))
    shp = jax.ShapeDtypeStruct((R, C), F32)
    grid_spec = pltpu.PrefetchScalarGridSpec(num_scalar_prefetch=1, grid=(2 * nbh,),
                                             in_specs=[blk, hblk, hblk, blk, blk], out_specs=[blk] * 4)
    return _pcall(body, name=name, grid_spec=grid_spec, out_shape=[shp] * 4,
                  compiler_params=_params("parallel"))(core, w, mine, theirs, m, v)


BIG = ("w_in", "w_dil_out", "w_fox_out", "w_out", "w_ffn_in", "w_ffn_down")
SMALL = ("norm_mix_g", "b_fgt", "b_gate", "norm_ffn_g", "norm_final_g")
ORDER = ("norm_mix_g", "w_in", "b_fgt", "b_gate", "w_dil_out", "w_fox_out", "w_out", "norm_ffn_g", "w_ffn_in",
         "w_ffn_down", "norm_final_g")
SMALL_ROWS = {"norm_mix_g": (0, 1), "b_gate": (1, 3), "norm_ffn_g": (3, 4), "norm_final_g": (4, 5), "b_fgt": (5, 6)}


def _columns_to_blocks(full, ncol):
    K = full.shape[0]
    return full.reshape(K, 4, ncol).transpose(1, 0, 2)


def _blocks_to_columns(blocks):
    n, K, ncol = blocks.shape
    return blocks.transpose(1, 0, 2).reshape(K, n * ncol)


def kernel(x, norm_mix_g, w_in, b_fgt, b_gate, w_dil_out, w_fox_out, w_out, norm_ffn_g, w_ffn_in, w_ffn_down, norm_final_g, loss_target, m_norm_mix_g, m_w_in, m_b_fgt, m_b_gate, m_w_dil_out, m_w_fox_out, m_w_out, m_norm_ffn_g, m_w_ffn_in, m_w_ffn_down, m_norm_final_g, v_norm_mix_g, v_w_in, v_b_fgt, v_b_gate, v_w_dil_out, v_w_fox_out, v_w_out, v_norm_ffn_g, v_w_ffn_in, v_w_ffn_down, v_norm_final_g):
    weights = dict(norm_mix_g=norm_mix_g, w_in=w_in, b_fgt=b_fgt, b_gate=b_gate, w_dil_out=w_dil_out,
                   w_fox_out=w_fox_out, w_out=w_out, norm_ffn_g=norm_ffn_g, w_ffn_in=w_ffn_in, w_ffn_down=w_ffn_down,
                   norm_final_g=norm_final_g)
    m_in = dict(norm_mix_g=m_norm_mix_g, w_in=m_w_in, b_fgt=m_b_fgt, b_gate=m_b_gate, w_dil_out=m_w_dil_out,
                w_fox_out=m_w_fox_out, w_out=m_w_out, norm_ffn_g=m_norm_ffn_g, w_ffn_in=m_w_ffn_in,
                w_ffn_down=m_w_ffn_down, norm_final_g=m_norm_final_g)
    v_in = dict(norm_mix_g=v_norm_mix_g, w_in=v_w_in, b_fgt=v_b_fgt, b_gate=v_b_gate, w_dil_out=v_w_dil_out,
                w_fox_out=v_w_fox_out, w_out=v_w_out, norm_ffn_g=v_norm_ffn_g, w_ffn_in=v_w_ffn_in,
                w_ffn_down=v_w_ffn_down, norm_final_g=v_norm_final_g)
    c = lax.axis_index("c")
    chip = 2 * lax.axis_index("x") + lax.axis_index("y")

    shards = {n: weights[n][0].astype(_CD) for n in BIG}
    in_shape = jax.ShapeDtypeStruct((4,) + shards["w_in"].shape, _CD)
    send_i, recv_i, in_src, in_land, token_in = _split_copy_start(
        [shards["w_in"]], [in_shape], _gather_copies, norm_mix_g, name="gather_in_start")
    late = BIG[1:]
    send_g, recv_g, late_src, late_land, token = _split_copy_start(
        [shards[n] for n in late], [jax.ShapeDtypeStruct((4,) + shards[n].shape, _CD) for n in late],
        _gather_whole_copies, token_in, name="gather_late_start")
    adam_in = [t[0] + token_in[0, 0] for t in (w_in, m_w_in, v_w_in)]
    p = dict(norm_mix_g=norm_mix_g, b_fgt=jnp.pad(b_fgt, ((0, 0), (0, F_PAD - N_FOX_HEADS))), b_gate=b_gate,
             norm_ffn_g=norm_ffn_g, norm_final_g=norm_final_g.reshape(1, D_MODEL))

    def first_weights(after):
        own, lands = _split_copy_wait(send_i, recv_i, in_src, in_land, _gather_copies, [after] + adam_in,
                                      name="gather_in_wait")
        (g_in,) = _forward_halves(lands, name="gather_in_forward")
        full_in = _blocks_to_columns(lax.dynamic_update_index_in_dim(g_in, own[0], chip, 0))
        o3 = QKV_COLS
        o4 = o3 + N_FOX_HEADS
        return dict(qkv=full_in[:, :o3], f=jnp.pad(full_in[:, o3:o4], ((0, 0), (0, F_PAD - N_FOX_HEADS))),
                    g=full_in[:, o4:])

    def late_weights(after):
        own, lands = _split_copy_wait(send_g, recv_g, late_src, late_land, _gather_whole_copies, after,
                                      name="gather_late_wait")
        g_dil, g_fox, g_out, g_ffn_in, g_ffn_down = [
            lax.dynamic_update_index_in_dim(l, s, chip, 0) for l, s in zip(lands, own)]
        return dict(dil_out=_blocks_to_columns(g_dil), fox_out=_blocks_to_columns(g_fox),
                    out=g_out.reshape(D_MODEL, D_MODEL), ffn_in=g_ffn_in,
                    ffn_down=g_ffn_down.reshape(D_FF, D_MODEL))

    def to_blocks(n, full):
        shape = weights[n].shape
        if full.ndim == 3:
            return full
        if n in ("w_out", "w_ffn_down"):
            return full.reshape(4, shape[1], shape[2])
        return _columns_to_blocks(full, shape[2])

    in_flight = {}

    def grad_sink(group, gw):
        if group == "in":
            named = {"w_in": jnp.concatenate([gw["qkv"], gw["f"][:, :N_FOX_HEADS], gw["g"]], axis=1)}
        else:
            named = {"w_" + k: v for k, v in gw.items()}
        srcs = [to_blocks(n, named[n]) for n in named]
        lands = [jax.ShapeDtypeStruct((7, s.shape[1] // 2, s.shape[2]), s.dtype) for s in srcs]
        started = _split_copy_start(srcs, lands, _scatter_all_copies, next(iter(gw.values())),
                                    name=f"scatter_{group}_start")
        in_flight[group] = (list(named), started)
        return started[-1]

    loss_part, grad_x, gw, small = _layer_step(x[0], loss_target[0], {}, p, late_weights, grad_sink,
                                               (token_in, token), first_weights)

    halves = {}
    where = jnp.stack([chip, c]).astype(jnp.int32)
    for group, (names, (send_s, recv_s, srcs, lands, _)) in in_flight.items():
        srcs, recv = _split_copy_wait(send_s, recv_s, srcs, lands, _scatter_all_copies, grad_x,
                                      name=f"scatter_{group}_wait")
        halves.update({n: _add_all(s, r, where, name=f"add_all_{n}") for n, s, r in zip(names, srcs, recv)})
    halves = [halves[n] for n in BIG]
    grad_halves = dict(zip(BIG, zip(halves, _share_halves(halves))))

    packed = jnp.concatenate([
        small["norm_mix_g"], small["b_gate"].reshape(2, D_MODEL), small["norm_ffn_g"], small["norm_final_g"],
        jnp.pad(small["b_fgt"], ((0, 0), (0, D_MODEL - F_PAD))), jnp.pad(loss_part, ((0, 0), (0, D_MODEL - 1))),
        jnp.zeros((1, D_MODEL), F32)], axis=0)
    summed = _sum_small(packed)
    loss = summed[6, 0]

    def pack_small(d):
        rows = [d["norm_mix_g"].reshape(1, D_MODEL), d["b_gate"].reshape(2, D_MODEL), d["norm_ffn_g"].reshape(1, D_MODEL),
                d["norm_final_g"].reshape(1, D_MODEL),
                jnp.pad(d["b_fgt"].reshape(1, N_FOX_HEADS), ((0, 0), (0, D_MODEL - N_FOX_HEADS))),
                jnp.zeros((2, D_MODEL), F32)]
        return jnp.concatenate(rows, axis=0)

    def unpack_small(t, n):
        lo, hi = SMALL_ROWS[n]
        return t[lo:hi].reshape(1, -1)[:, :weights[n].size].reshape(weights[n].shape)

    small_out = (summed,) + tuple(_adamw(pack_small(weights), summed, pack_small(m_in), pack_small(v_in),
                                         name="adamw_small"))
    out_g, out_d, out_m, out_v = {}, {}, {}, {}
    core = jnp.reshape(c, (1,)).astype(jnp.int32)
    for n in SMALL:
        out_g[n], out_d[n], out_m[n], out_v[n] = [unpack_small(t, n) for t in small_out]
    for n in BIG:
        shape = weights[n].shape
        wmv = adam_in if n == "w_in" else [t[0] for t in (weights[n], m_in[n], v_in[n])]
        mine, theirs = grad_halves[n]
        outs = _adamw_halves(wmv[0], mine, theirs, wmv[1], wmv[2], core, name=f"adamw_{n}")
        out_g[n], out_d[n], out_m[n], out_v[n] = [t.reshape(shape) for t in outs]
    return (loss, grad_x[None], *[out_g[n] for n in ORDER], *[out_d[n] for n in ORDER],
            *[out_m[n] for n in ORDER], *[out_v[n] for n in ORDER])
```

```python
import numpy as np
import jax
import jax.numpy as jnp
from jax import lax
from jax.experimental import pallas as pl
from jax.experimental.pallas import tpu as pltpu

F32 = jnp.float32
_CD = jnp.bfloat16

D_MODEL = 1024
HEAD_DIM = 64
DIL_PAIRS = ((128, 1), (512, 4), (2048, 16))
N_DIL_GROUPS = 3
DIL_HEADS = 4
DIL_W = 128
DIL_OUT = DIL_HEADS * HEAD_DIM
DIL_WIDTH = N_DIL_GROUPS * DIL_OUT
N_FOX_HEADS = 8
FOX_WIDTH = N_FOX_HEADS * HEAD_DIM
D_FF = 2816
QKV_COLS = 3 * DIL_WIDTH + 3 * FOX_WIDTH
F_PAD = 128
RMS_EPS = 1e-6
NEG_INF = -1e30
ATTN_SCALE = HEAD_DIM ** -0.5
ADAM_LR, ADAM_B1, ADAM_B2, ADAM_EPS, ADAM_WD, ADAM_STEP = 0.001, 0.9, 0.999, 1e-08, 0.01, 10

VMEM_LIMIT = 48 * 1024 * 1024
VMEM_LIMIT_RESIDENT = 56 * 1024 * 1024
LANES = 128
MESH = pl.DeviceIdType.MESH
HBM_SPEC = pl.BlockSpec(memory_space=pltpu.HBM)


def _pcall(body, after=None, **kw):
    if after is None:
        return pl.pallas_call(body, **kw)
    n_in = len(kw["in_specs"])
    kw["in_specs"] = list(kw["in_specs"]) + [pl.BlockSpec(memory_space=pl.ANY)]

    def tied(*refs):
        return body(*refs[:n_in], *refs[n_in + 1:])

    call = pl.pallas_call(tied, **kw)
    return lambda *args: call(*args, after)


def _params(*sem):
    return pltpu.CompilerParams(dimension_semantics=sem, vmem_limit_bytes=VMEM_LIMIT)


def _pick(dim, pref):
    t = (min(pref, dim) // 128) * 128
    while t >= 128:
        if dim % t == 0:
            return t
        t -= 128
    return dim


def _mm(a, b, *, name, ta=False, tb=False, out_dtype=F32, add=None, tm=1024, tn=512, tk=2048, after=None,
        b_blocks=False, out_blocks=None, a_halves=False, b_halves=False, rms_bwd=None):
    if a_halves:
        M, K = a.shape[1], 2 * a.shape[2]
    elif ta:
        K, M = a.shape
    else:
        M, K = a.shape
    if b_halves:
        b_rows, b_cols = b.shape[1], 2 * b.shape[2]
    else:
        b_rows, b_cols = (b.shape[1], b.shape[0] * b.shape[2]) if b_blocks else b.shape
    if tb:
        N, K2 = b_rows, b_cols
    else:
        K2, N = b_rows, b_cols
    assert K == K2, (a.shape, b.shape)
    shard = b.shape[2] if b_blocks else None
    tm = _pick(M, tm)
    tn = _pick(shard if (b_blocks and not tb) else (out_blocks or N), tn)
    tk = _pick(shard if (b_blocks and tb) else K, tk)
    nk = K // tk
    dn = (((0 if ta else 1,), (1 if tb else 0,)), ((), ()))
    has_add = add is not None
    assert not (has_add and out_blocks)
    has_norm = rms_bwd is not None
    if has_norm:
        tn = N
        assert not out_blocks and out_dtype == F32

    def body(*refs):
        a_ref, b_ref = refs[0], refs[1]
        rest = list(refs[2:])
        add_ref = rest.pop(0) if has_add else None
        x_ref, g_ref, dres_ref = (rest.pop(0), rest.pop(0), rest.pop(0)) if has_norm else (None, None, None)
        o_ref = rest.pop(0)
        dg_ref = rest.pop(0) if has_norm else None
        bv = b_ref[0] if b_blocks else b_ref[...]
        p = lax.dot_general(a_ref[...].astype(_CD), bv.astype(_CD), dn, preferred_element_type=F32)

        def finish(r):
            if has_add:
                r = r + add_ref[...]
            if has_norm:
                xv = x_ref[...]
                rs = lax.rsqrt(jnp.mean(xv * xv, axis=-1, keepdims=True) + RMS_EPS)
                xh = xv * rs
                dxh = r * g_ref[...]
                o_ref[...] = dres_ref[...] + rs * (dxh - xh * jnp.mean(dxh * xh, axis=-1, keepdims=True))
                part = jnp.sum(r * xh, axis=0, keepdims=True)
                first = pl.program_id(0) == 0

                @pl.when(first)
                def _():
                    dg_ref[...] = part

                @pl.when(jnp.logical_not(first))
                def _():
                    dg_ref[...] += part
            elif out_blocks:
                o_ref[0] = r.astype(out_dtype)
            else:
                o_ref[...] = r.astype(out_dtype)

        if nk == 1:
            finish(p)
        else:
            acc_ref = rest.pop(0)
            k = pl.program_id(2)

            @pl.when(k == 0)
            def _():
                acc_ref[...] = p

            @pl.when(k > 0)
            def _():
                acc_ref[...] += p

            @pl.when(k == nk - 1)
            def _():
                finish(acc_ref[...])

    if a_halves:
        ka = (K // 2) // tk
        a_spec = pl.BlockSpec((None, tm, tk), lambda i, j, k: (k // ka, i, k % ka))
    else:
        a_spec = pl.BlockSpec((tk, tm), lambda i, j, k: (k, i)) if ta else pl.BlockSpec((tm, tk), lambda i, j, k: (i, k))
    if b_halves:
        nb_ = (N // 2) // tn
        b_spec = pl.BlockSpec((None, tk, tn), lambda i, j, k: (j // nb_, k, j % nb_))
    elif b_blocks and tb:
        per = shard // tk
        b_spec = pl.BlockSpec((1, tn, tk), lambda i, j, k: (k // per, j, k % per))
    elif b_blocks:
        per = shard // tn
        b_spec = pl.BlockSpec((1, tk, tn), lambda i, j, k: (j // per, k, j % per))
    else:
        b_spec = pl.BlockSpec((tn, tk), lambda i, j, k: (j, k)) if tb else pl.BlockSpec((tk, tn), lambda i, j, k: (k, j))
    if out_blocks:
        oper = out_blocks // tn
        o_spec = pl.BlockSpec((1, tm, tn), lambda i, j, k: (j // oper, i, j % oper))
        out_shape = jax.ShapeDtypeStruct((N // out_blocks, M, out_blocks), out_dtype)
    else:
        o_spec = pl.BlockSpec((tm, tn), lambda i, j, k: (i, j))
        out_shape = jax.ShapeDtypeStruct((M, N), out_dtype)
    in_specs = [a_spec, b_spec] + ([o_spec] if has_add else [])
    args = (a, b) + ((add,) if has_add else ())
    out_specs, semantics = o_spec, ("parallel", "parallel", "arbitrary")
    if has_norm:
        vec = pl.BlockSpec((1, N), lambda i, j, k: (0, 0))
        in_specs += [o_spec, vec, o_spec]
        args += tuple(rms_bwd)
        out_specs, out_shape = [o_spec, vec], [out_shape, jax.ShapeDtypeStruct((1, N), F32)]
        semantics = ("arbitrary", "arbitrary", "arbitrary")
    return _pcall(
        body, after, name=name, grid=(M // tm, N // tn, nk), in_specs=in_specs, out_specs=out_specs,
        out_shape=out_shape,
        scratch_shapes=[pltpu.VMEM((tm, tn), F32)] if nk > 1 else [],
        compiler_params=_params(*semantics),
    )(*args)


def _rms_fwd(x, g, *, name, tm=512, after=None):
    S, D = x.shape

    def body(x_ref, g_ref, h_ref):
        xv = x_ref[...]
        r = lax.rsqrt(jnp.mean(xv * xv, axis=-1, keepdims=True) + RMS_EPS)
        h_ref[...] = ((xv * r) * g_ref[...]).astype(h_ref.dtype)

    row = pl.BlockSpec((tm, D), lambda i: (i, 0))
    return _pcall(body, after, name=name, grid=(S // tm,), in_specs=[row, pl.BlockSpec((1, D), lambda i: (0, 0))],
                  out_specs=row, out_shape=jax.ShapeDtypeStruct((S, D), _CD), compiler_params=_params("parallel"))(x, g)


def _ffn_down_loss(act, w_down, x1, g, tgt, *, name, tm=512):
    S, D = x1.shape
    F = act.shape[1]

    def body(a_ref, b_ref, x_ref, g_ref, t_ref, loss_ref, dx_ref, dg_ref):
        xv = x_ref[...] + jnp.dot(a_ref[...].astype(_CD), b_ref[...].astype(_CD), preferred_element_type=F32)
        gv = g_ref[...]
        r = lax.rsqrt(jnp.mean(xv * xv, axis=-1, keepdims=True) + RMS_EPS)
        xh = xv * r
        err = xh * gv - t_ref[...]
        lpart = 0.5 * jnp.sum(jnp.mean(err * err, axis=-1, keepdims=True), axis=0, keepdims=True)
        dy = err * (1.0 / D)
        dxh = dy * gv
        dx_ref[...] = r * (dxh - xh * jnp.mean(dxh * xh, axis=-1, keepdims=True))
        gpart = jnp.sum(dy * xh, axis=0, keepdims=True)

        @pl.when(pl.program_id(0) == 0)
        def _():
            loss_ref[...] = lpart
            dg_ref[...] = gpart

        @pl.when(pl.program_id(0) > 0)
        def _():
            loss_ref[...] += lpart
            dg_ref[...] += gpart

    row = pl.BlockSpec((tm, D), lambda i: (i, 0))
    vec = pl.BlockSpec((1, D), lambda i: (0, 0))
    one = pl.BlockSpec((1, 1), lambda i: (0, 0))
    return _pcall(body, name=name, grid=(S // tm,),
                  in_specs=[pl.BlockSpec((tm, F), lambda i: (i, 0)), pl.BlockSpec((F, D), lambda i: (0, 0)), row, vec, row],
                  out_specs=[one, row, vec],
                  out_shape=[jax.ShapeDtypeStruct((1, 1), F32), jax.ShapeDtypeStruct((S, D), F32),
                             jax.ShapeDtypeStruct((1, D), F32)],
                  compiler_params=_params("arbitrary"))(act, w_down, x1, g, tgt)


def _sigmoid(z):
    return 1.0 / (1.0 + jnp.exp(-z))


def _gated_mix_out(gl, bg, ya, yb, w_out, x, g, *, name, tm=512):
    S, D = ya.shape

    def body(za_ref, zb_ref, ba_ref, bb_ref, ya_ref, yb_ref, w_ref, x_ref, g_ref, m_ref, x1_ref, h_ref):
        ga = _sigmoid(za_ref[...].astype(F32) + ba_ref[...])
        gb = _sigmoid(zb_ref[...].astype(F32) + bb_ref[...])
        merged = (ga * ya_ref[...].astype(F32) + gb * yb_ref[...].astype(F32)).astype(m_ref.dtype)
        m_ref[...] = merged
        x1 = x_ref[...] + jnp.dot(merged, w_ref[...].astype(_CD), preferred_element_type=F32)
        x1_ref[...] = x1
        rs = lax.rsqrt(jnp.mean(x1 * x1, axis=-1, keepdims=True) + RMS_EPS)
        h_ref[...] = ((x1 * rs) * g_ref[...]).astype(h_ref.dtype)

    lo = pl.BlockSpec((tm, D), lambda i: (i, 0))
    hi = pl.BlockSpec((tm, D), lambda i: (i, 1))
    vlo = pl.BlockSpec((1, D), lambda i: (0, 0))
    vhi = pl.BlockSpec((1, D), lambda i: (0, 1))
    whole = pl.BlockSpec((D, D), lambda i: (0, 0))
    return _pcall(body, name=name, grid=(S // tm,), in_specs=[lo, hi, vlo, vhi, lo, lo, whole, lo, vlo],
                  out_specs=[lo, lo, lo],
                  out_shape=[jax.ShapeDtypeStruct((S, D), _CD), jax.ShapeDtypeStruct((S, D), F32),
                             jax.ShapeDtypeStruct((S, D), _CD)],
                  compiler_params=_params("parallel"))(gl, gl, bg, bg, ya, yb, w_out, x, g)


def _gate_bwd(dm, gl, bg, ya, yb, *, name, tm=256):
    S, D = ya.shape

    def body(dm_ref, za_ref, zb_ref, ba_ref, bb_ref, ya_ref, yb_ref, dya_ref, dyb_ref, dgl_ref, dbg_ref):
        dmv = dm_ref[...].astype(F32)
        ga = _sigmoid(za_ref[...].astype(F32) + ba_ref[...])
        gb = _sigmoid(zb_ref[...].astype(F32) + bb_ref[...])
        dya_ref[...] = (dmv * ga).astype(dya_ref.dtype)
        dyb_ref[...] = (dmv * gb).astype(dyb_ref.dtype)
        dza = dmv * ya_ref[...].astype(F32) * ga * (1.0 - ga)
        dzb = dmv * yb_ref[...].astype(F32) * gb * (1.0 - gb)
        dgl_ref[:, :D] = dza.astype(dgl_ref.dtype)
        dgl_ref[:, D:] = dzb.astype(dgl_ref.dtype)
        pa = jnp.sum(dza, axis=0, keepdims=True)
        pb = jnp.sum(dzb, axis=0, keepdims=True)

        @pl.when(pl.program_id(0) == 0)
        def _():
            dbg_ref[:, :D] = pa
            dbg_ref[:, D:] = pb

        @pl.when(pl.program_id(0) > 0)
        def _():
            dbg_ref[:, :D] += pa
            dbg_ref[:, D:] += pb

    lo = pl.BlockSpec((tm, D), lambda i: (i, 0))
    hi = pl.BlockSpec((tm, D), lambda i: (i, 1))
    vlo = pl.BlockSpec((1, D), lambda i: (0, 0))
    vhi = pl.BlockSpec((1, D), lambda i: (0, 1))
    wide = pl.BlockSpec((tm, 2 * D), lambda i: (i, 0))
    vwide = pl.BlockSpec((1, 2 * D), lambda i: (0, 0))
    return _pcall(body, name=name, grid=(S // tm,), in_specs=[lo, lo, hi, vlo, vhi, lo, lo],
                  out_specs=[lo, lo, wide, vwide],
                  out_shape=[jax.ShapeDtypeStruct((S, D), _CD), jax.ShapeDtypeStruct((S, D), _CD),
                             jax.ShapeDtypeStruct((S, 2 * D), _CD), jax.ShapeDtypeStruct((1, 2 * D), F32)],
                  compiler_params=_params("arbitrary"))(dm, gl, gl, bg, bg, ya, yb)


def _ffn_in_act(h2, w_blocks, *, name, tm=512):
    S, D = h2.shape
    _, _, C = w_blocks.shape

    def body(a_ref, bg_ref, bu_ref, g_ref, u_ref, o_ref):
        av = a_ref[...].astype(_CD)
        gv = jnp.dot(av, bg_ref[0].astype(_CD), preferred_element_type=F32)
        uv = jnp.dot(av, bu_ref[0].astype(_CD), preferred_element_type=F32)
        g_ref[...] = gv.astype(g_ref.dtype)
        u_ref[...] = uv.astype(u_ref.dtype)
        o_ref[...] = (gv * _sigmoid(gv) * uv).astype(o_ref.dtype)

    out = pl.BlockSpec((tm, C), lambda i, j: (i, j))
    shp = jax.ShapeDtypeStruct((S, 2 * C), _CD)
    return _pcall(body, name=name, grid=(S // tm, 2),
                  in_specs=[pl.BlockSpec((tm, D), lambda i, j: (i, 0)), pl.BlockSpec((1, D, C), lambda i, j: (j, 0, 0)),
                            pl.BlockSpec((1, D, C), lambda i, j: (2 + j, 0, 0))],
                  out_specs=[out, out, out], out_shape=[shp, shp, shp],
                  compiler_params=_params("parallel", "arbitrary"))(h2, w_blocks, w_blocks)


def _d_swiglu(dx, w_down, gate, up, *, name, tm=512, tn=1408):
    S, D = dx.shape
    F = w_down.shape[0]
    nt = (((1,), (1,)), ((), ()))

    def body(a_ref, b_ref, g_ref, u_ref, o_ref):
        dv = lax.dot_general(a_ref[...].astype(_CD), b_ref[...].astype(_CD), nt, preferred_element_type=F32)
        gv = g_ref[...].astype(F32)
        sg = _sigmoid(gv)
        o_ref[0] = (dv * u_ref[...].astype(F32) * (sg * (1.0 + gv * (1.0 - sg)))).astype(o_ref.dtype)
        o_ref[1] = (dv * (gv * sg)).astype(o_ref.dtype)

    tile = pl.BlockSpec((tm, tn), lambda i, j: (i, j))
    return _pcall(body, name=name, grid=(S // tm, F // tn),
                  in_specs=[pl.BlockSpec((tm, D), lambda i, j: (i, 0)), pl.BlockSpec((tn, D), lambda i, j: (j, 0)),
                            tile, tile],
                  out_specs=pl.BlockSpec((2, tm, tn), lambda i, j: (0, i, j)),
                  out_shape=jax.ShapeDtypeStruct((2, S, F), _CD),
                  compiler_params=_params("parallel", "arbitrary"))(dx, w_down, gate, up)


def _split3(x):
    hi = x.astype(jnp.bfloat16)
    r1 = x - hi.astype(F32)
    mid = r1.astype(jnp.bfloat16)
    lo = (r1 - mid.astype(F32)).astype(jnp.bfloat16)
    return hi, mid, lo


def _ones_dot_left(ones, x):
    return sum(jnp.dot(ones, p, preferred_element_type=F32) for p in _split3(x))


def _ones_dot_right(x, ones):
    return sum(jnp.dot(p, ones, preferred_element_type=F32) for p in _split3(x))


def _head_sum(x):
    n = x.shape[1]
    r = lax.broadcasted_iota(jnp.int32, (n, n), 0) // HEAD_DIM
    c = lax.broadcasted_iota(jnp.int32, (n, n), 1) // HEAD_DIM
    return _ones_dot_right(x, (r == c).astype(jnp.bfloat16))


def _log_sigmoid(z):
    e = jnp.exp(-jnp.abs(z))
    t = 1.0 + e
    log1p_e = jnp.where(t == 1.0, e, jnp.log(t) * (e / jnp.where(t == 1.0, 1.0, t - 1.0)))
    return jnp.minimum(z, 0.0) - log1p_e


def _fox_cumsum(zf, bf, *, name):
    S, W = zf.shape
    nb = S // 128

    def body(z_ref, b_ref, c_ref):
        tri = (lax.broadcasted_iota(jnp.int32, (128, 128), 0) >= lax.broadcasted_iota(jnp.int32, (128, 128), 1))
        tri = tri.astype(jnp.bfloat16)

        def step(i, carry):
            rows = pl.ds(pl.multiple_of(i * 128, 128), 128)
            lf = _log_sigmoid(z_ref[rows, :] + b_ref[...])
            cb = _ones_dot_left(tri, lf) + carry
            c_ref[rows, :] = cb
            return cb[127:128, :]

        lax.fori_loop(0, nb, step, jnp.zeros((1, W), F32))

    return _pcall(body, name=name, out_shape=jax.ShapeDtypeStruct((S, W), F32),
                  compiler_params=pltpu.CompilerParams(vmem_limit_bytes=VMEM_LIMIT))(zf, bf)


def _fox_cumsum_bwd(dc, zf, bf, *, name):
    S, W = zf.shape
    nb = S // 128

    def body(dc_ref, z_ref, b_ref, dz_ref, db_ref):
        tri = (lax.broadcasted_iota(jnp.int32, (128, 128), 0) <= lax.broadcasted_iota(jnp.int32, (128, 128), 1))
        tri = tri.astype(jnp.bfloat16)

        def step(k, carry):
            tail, acc = carry
            i = nb - 1 - k
            rows = pl.ds(pl.multiple_of(i * 128, 128), 128)
            dlf = _ones_dot_left(tri, dc_ref[rows, :]) + tail
            dz = dlf * _sigmoid(-(z_ref[rows, :] + b_ref[...]))
            dz_ref[rows, :] = dz
            return dlf[0:1, :], acc + jnp.sum(dz, axis=0, keepdims=True)

        _, acc = lax.fori_loop(0, nb, step, (jnp.zeros((1, W), F32), jnp.zeros((1, W), F32)))
        db_ref[...] = acc

    return _pcall(body, name=name,
                  out_shape=[jax.ShapeDtypeStruct((S, W), F32), jax.ShapeDtypeStruct((1, W), F32)],
                  compiler_params=pltpu.CompilerParams(vmem_limit_bytes=VMEM_LIMIT))(dc, zf, bf)


def _proj_dil(h, w_qkv, *, name, tm=1024):
    S, D = h.shape
    tn = DIL_WIDTH

    def body(a_ref, b_ref, *rest):
        outs, acc = rest[:N_DIL_GROUPS], rest[N_DIL_GROUPS]
        prod = jnp.dot(a_ref[...].astype(_CD), b_ref[...].astype(_CD), preferred_element_type=F32)
        for k in range(tn // LANES):
            acc[k] = prod[:, k * LANES:(k + 1) * LANES]
        for g, (_, d) in enumerate(DIL_PAIRS):
            for half in range(DIL_OUT // LANES):
                k = g * (DIL_OUT // LANES) + half
                cols = slice(half * LANES, (half + 1) * LANES)
                for r in range(d):
                    rows = pl.ds(r, tm // d, stride=d) if d > 1 else slice(None)
                    outs[g][0, r, :, cols] = acc[k, rows, :].astype(outs[g].dtype)

    out_specs = [pl.BlockSpec((1, d, tm // d, DIL_OUT), lambda i, j: (j, 0, i, 0)) for _, d in DIL_PAIRS]
    out_shape = [jax.ShapeDtypeStruct((3, d, S // d, DIL_OUT), _CD) for _, d in DIL_PAIRS]
    outs = _pcall(body, name=name, grid=(S // tm, 3),
                  in_specs=[pl.BlockSpec((tm, D), lambda i, j: (i, 0)), pl.BlockSpec((D, tn), lambda i, j: (0, j))],
                  out_specs=out_specs, out_shape=out_shape, scratch_shapes=[pltpu.VMEM((tn // LANES, tm, LANES), F32)],
                  compiler_params=_params("parallel", "arbitrary"))(h, w_qkv)
    return [o.reshape(3, S, DIL_OUT) for o in outs]


def _dil_start(block, S, dilation):
    sub = S // dilation
    u0 = block * DIL_W
    return (u0 % sub) * dilation + u0 // sub


def _dil_slopes(group):
    h = np.arange(1, N_DIL_GROUPS * DIL_HEADS + 1, dtype=np.float32)
    s = (np.float32(2.0) ** (np.float32(-8.0) * h / np.float32(N_DIL_GROUPS * DIL_HEADS))).astype(np.float32)
    return [float(v) for v in s.reshape(N_DIL_GROUPS, DIL_HEADS)[group]]


def _dil_tiles(i, n, blocks_per_seq):
    qi = lax.broadcasted_iota(jnp.int32, (DIL_W, 2 * DIL_W), 0)
    kj = lax.broadcasted_iota(jnp.int32, (DIL_W, 2 * DIL_W), 1)
    rel = qi + DIL_W - kj
    first = ((4 * n + i) % blocks_per_seq) == 0
    valid = jnp.logical_and(jnp.logical_and(rel >= 0, rel <= DIL_W), jnp.logical_or(kj >= DIL_W, jnp.logical_not(first)))
    return valid, rel.astype(F32)


def _dil_window(cur_ref, prev_ref, i, cols):
    if i > 0:
        return cur_ref[(i - 1) * DIL_W:(i + 1) * DIL_W, cols]
    return jnp.concatenate([prev_ref[:, cols], cur_ref[:DIL_W, cols]], axis=0)


CHUNK = 4 * DIL_W


def _dil_rows(block, S, dilation):
    start = _dil_start(block, S, dilation)
    return pl.ds(start, DIL_W, stride=dilation) if dilation > 1 else pl.ds(start, DIL_W)


def SPLIT(S):
    return (DIL_OUT // LANES, S, LANES)


def _dil_fwd(qkv, group, *, name):
    S = qkv.shape[1]
    dilation = DIL_PAIRS[group][1]
    bps = (S // dilation) // DIL_W
    slopes = _dil_slopes(group)
    nt = (((1,), (1,)), ((), ()))

    def body(q_ref, k_ref, v_ref, kp_ref, vp_ref, on_ref, ln_ref, o_ref, l_ref):
        n = pl.program_id(0)
        for i in range(4):
            valid, rel = _dil_tiles(i, n, bps)
            rows = slice(i * DIL_W, (i + 1) * DIL_W)
            for h in range(DIL_HEADS):
                cols = slice(h * HEAD_DIM, (h + 1) * HEAD_DIM)
                qh = q_ref[rows, cols]
                k2, v2 = _dil_window(k_ref, kp_ref, i, cols), _dil_window(v_ref, vp_ref, i, cols)
                s = lax.dot_general(qh, k2, nt, preferred_element_type=F32) * ATTN_SCALE - (slopes[h] * dilation) * rel
                s = jnp.where(valid, s, NEG_INF)
                m = jnp.max(s, axis=-1, keepdims=True)
                p = jnp.exp(s - m)
                den = jnp.sum(p, axis=-1, keepdims=True)
                acc = jnp.dot(p.astype(_CD), v2, preferred_element_type=F32)
                o_ref[rows, cols] = acc / den
                l_ref[rows, cols] = jnp.broadcast_to(m + jnp.log(den), (DIL_W, HEAD_DIM))
        for i in range(4):
            rows = slice(i * DIL_W, (i + 1) * DIL_W)
            nat = _dil_rows(4 * n + i, S, dilation)
            for half in range(DIL_OUT // LANES):
                cols = slice(half * LANES, (half + 1) * LANES)
                on_ref[half, nat, :] = o_ref[rows, cols]
                ln_ref[half, nat, :] = l_ref[rows, cols]

    def cur(which):
        return pl.BlockSpec((None, CHUNK, DIL_OUT), lambda n: (which, n, 0))

    def prev(which):
        return pl.BlockSpec((None, DIL_W, DIL_OUT), lambda n: (which, jnp.maximum(4 * n - 1, 0), 0))

    whole = pl.BlockSpec(SPLIT(S), lambda n: (0, 0, 0))
    return _pcall(body, name=name, grid=(S // CHUNK,), in_specs=[cur(0), cur(1), cur(2), prev(1), prev(2)],
                  out_specs=[whole, whole],
                  out_shape=[jax.ShapeDtypeStruct(SPLIT(S), F32), jax.ShapeDtypeStruct(SPLIT(S), F32)],
                  scratch_shapes=[pltpu.VMEM((CHUNK, DIL_OUT), F32), pltpu.VMEM((CHUNK, DIL_OUT), F32)],
                  compiler_params=_params("arbitrary"))(qkv, qkv, qkv, qkv, qkv)


STAT_OFFSET = HEAD_DIM // 2


def _dil_bwd(qkv, stats, do, group, *, name):
    S = qkv.shape[1]
    dilation = DIL_PAIRS[group][1]
    bps = (S // dilation) // DIL_W
    slopes = _dil_slopes(group)
    nchunk = S // CHUNK
    nt = (((1,), (1,)), ((), ()))
    tn = (((0,), (0,)), ((), ()))

    def body(q_ref, k_ref, v_ref, kp_ref, vp_ref, ln_ref, don_ref, dqn_ref, dkn_ref, dvn_ref,
             dk_s, dv_s, l_ref, do_ref, dq_ref):
        step = pl.program_id(0)
        n = nchunk - 1 - step
        for i in range(4):
            rows = slice(i * DIL_W, (i + 1) * DIL_W)
            nat = _dil_rows(4 * n + i, S, dilation)
            for half in range(DIL_OUT // LANES):
                cols = slice(half * LANES, (half + 1) * LANES)
                l_ref[rows, cols] = ln_ref[half, nat, :]
                do_ref[rows, cols] = don_ref[half, nat, :]

        @pl.when(step == 0)
        def _():
            dk_s[:, CHUNK:] = jnp.zeros((DIL_OUT, DIL_W), F32)
            dv_s[:, CHUNK:] = jnp.zeros((DIL_OUT, DIL_W), F32)

        dk_s[:, :CHUNK] = jnp.zeros((DIL_OUT, CHUNK), F32)
        dv_s[:, :CHUNK] = jnp.zeros((DIL_OUT, CHUNK), F32)
        for i in range(4):
            valid, rel = _dil_tiles(i, n, bps)
            rows = slice(i * DIL_W, (i + 1) * DIL_W)
            window = slice(i * DIL_W, (i + 2) * DIL_W)
            for h in range(DIL_HEADS):
                cols = slice(h * HEAD_DIM, (h + 1) * HEAD_DIM)
                qh = q_ref[rows, cols]
                k2, v2 = _dil_window(k_ref, kp_ref, i, cols), _dil_window(v_ref, vp_ref, i, cols)
                lh = l_ref[rows, h * HEAD_DIM:h * HEAD_DIM + 1]
                shift = l_ref[rows, h * HEAD_DIM + STAT_OFFSET:h * HEAD_DIM + STAT_OFFSET + 1]
                s = lax.dot_general(qh, k2, nt, preferred_element_type=F32) * ATTN_SCALE - (slopes[h] * dilation) * rel
                p = jnp.exp(jnp.where(valid, s, NEG_INF) - lh)
                dob = do_ref[rows, cols].astype(_CD)
                ds = p * (lax.dot_general(dob, v2, nt, preferred_element_type=F32) + shift)
                dsb = (ds * ATTN_SCALE).astype(_CD)
                dq_ref[rows, cols] = jnp.dot(dsb, k2, preferred_element_type=F32)
                dk_s[cols, window] += lax.dot_general(qh, dsb, tn, preferred_element_type=F32)
                dv_s[cols, window] += lax.dot_general(dob, p.astype(_CD), tn, preferred_element_type=F32)
        for i in range(4):
            rows = slice(i * DIL_W, (i + 1) * DIL_W)
            done = slice((i + 1) * DIL_W, (i + 2) * DIL_W)
            nat = _dil_rows(4 * n + i, S, dilation)
            dkb, dvb = dk_s[:, done].T, dv_s[:, done].T
            for half in range(DIL_OUT // LANES):
                cols = slice(half * LANES, (half + 1) * LANES)
                dqn_ref[half, nat, :] = dq_ref[rows, cols]
                dkn_ref[half, nat, :] = dkb[:, cols]
                dvn_ref[half, nat, :] = dvb[:, cols]
        dk_s[:, CHUNK:] = dk_s[:, :DIL_W]
        dv_s[:, CHUNK:] = dv_s[:, :DIL_W]

    def cur(which):
        return pl.BlockSpec((None, CHUNK, DIL_OUT), lambda s: (which, nchunk - 1 - s, 0))

    def prev(which):
        return pl.BlockSpec((None, DIL_W, DIL_OUT), lambda s: (which, jnp.maximum(4 * (nchunk - 1 - s) - 1, 0), 0))

    whole = pl.BlockSpec(SPLIT(S), lambda s: (0, 0, 0))
    shp = jax.ShapeDtypeStruct(SPLIT(S), F32)
    tile = pltpu.VMEM((CHUNK, DIL_OUT), F32)
    return _pcall(body, name=name, grid=(nchunk,),
                  in_specs=[cur(0), cur(1), cur(2), prev(1), prev(2), whole, whole],
                  out_specs=[whole, whole, whole], out_shape=[shp, shp, shp],
                  scratch_shapes=[pltpu.VMEM((DIL_OUT, CHUNK + DIL_W), F32), pltpu.VMEM((DIL_OUT, CHUNK + DIL_W), F32),
                                  tile, tile, tile],
                  compiler_params=pltpu.CompilerParams(dimension_semantics=("arbitrary",),
                                                       vmem_limit_bytes=VMEM_LIMIT_RESIDENT))(
        qkv, qkv, qkv, qkv, qkv, stats, do)


def _dil_mix_fwd(os_, ls_, *, name, tm=512):
    nh, S, _ = os_[0].shape

    def body(o0, o1, o2, l0, l1, l2, out_ref):
        for half in range(nh):
            ls = [l0[half], l1[half], l2[half]]
            m = jnp.maximum(jnp.maximum(ls[0], ls[1]), ls[2])
            es = [jnp.exp(l - m) for l in ls]
            den = es[0] + es[1] + es[2]
            mixed = (es[0] * o0[half] + es[1] * o1[half] + es[2] * o2[half]) / den
            out_ref[:, half * LANES:(half + 1) * LANES] = mixed.astype(out_ref.dtype)

    halves = pl.BlockSpec((nh, tm, LANES), lambda i: (0, i, 0))
    row = pl.BlockSpec((tm, nh * LANES), lambda i: (i, 0))
    return _pcall(body, name=name, grid=(S // tm,), in_specs=[halves] * 6, out_specs=row,
                  out_shape=jax.ShapeDtypeStruct((S, nh * LANES), _CD), compiler_params=_params("parallel"))(*os_, *ls_)


def _dil_mix_bwd(doa, os_, ls_, *, name, tm=512, after=None):
    nh, S, _ = os_[0].shape

    def body(d_ref, o0, o1, o2, l0, l1, l2, do0, do1, do2, st0, st1, st2):
        first = lax.broadcasted_iota(jnp.int32, (tm, LANES), 1) % HEAD_DIM < STAT_OFFSET
        for half in range(nh):
            dv = d_ref[:, half * LANES:(half + 1) * LANES]
            ls = [l0[half], l1[half], l2[half]]
            m = jnp.maximum(jnp.maximum(ls[0], ls[1]), ls[2])
            es = [jnp.exp(l - m) for l in ls]
            den = es[0] + es[1] + es[2]
            al = [e / den for e in es]
            da = [_head_sum(dv * o[half]) for o in (o0, o1, o2)]
            mean = al[0] * da[0] + al[1] * da[1] + al[2] * da[2]
            for a, l, do_ref, st_ref in zip(al, ls, (do0, do1, do2), (st0, st1, st2)):
                do_ref[half] = a * dv
                st_ref[half] = jnp.where(first, l, -a * mean)

    halves = pl.BlockSpec((nh, tm, LANES), lambda i: (0, i, 0))
    row = pl.BlockSpec((tm, nh * LANES), lambda i: (i, 0))
    shp = jax.ShapeDtypeStruct((nh, S, LANES), F32)
    return _pcall(body, after, name=name, grid=(S // tm,), in_specs=[row] + [halves] * 6, out_specs=[halves] * 6,
                  out_shape=[shp] * 6, compiler_params=_params("parallel"))(doa, *os_, *ls_)


FOX_T = 512


PACK = 2 * HEAD_DIM
HEAD_PAIRS = N_FOX_HEADS // 2
FOX_HPS = 8
Q_BLOCK0 = 0
K_BLOCK0 = FOX_WIDTH // PACK
V_BLOCK0 = 2 * FOX_WIDTH // PACK


def _pieces(x):
    hi = x.astype(jnp.bfloat16).astype(F32)
    r = x - hi
    mid = r.astype(jnp.bfloat16).astype(F32)
    lo = (r - mid).astype(jnp.bfloat16).astype(F32)
    return [hi, mid, lo]


def _extras(first, second, rows):
    lane = lax.broadcasted_iota(jnp.int32, (rows, HEAD_DIM), 1)
    out = jnp.zeros((rows, HEAD_DIM), F32)
    for base, triple in ((0, first), (3, second)):
        if all(isinstance(v, float) for v in triple) and len(set(triple)) == 1:
            if triple[0] != 0.0:
                out = jnp.where(jnp.logical_and(lane >= base, lane < base + 3), triple[0], out)
        else:
            for idx, val in enumerate(triple):
                out = jnp.where(lane == base + idx, val, out)
    return out


def _head_column(c, h):
    lane = lax.broadcasted_iota(jnp.int32, c.shape, 1)
    return jnp.sum(jnp.where(lane == h, c, 0.0), axis=1, keepdims=True)


ONES3 = [1.0, 1.0, 1.0]
ZEROS3 = [0.0, 0.0, 0.0]


def _fox_pack_fwd(qkv, c, *, name, tm=512):
    S = qkv.shape[0]

    def body(q_ref, k_ref, v_ref, c_ref, qo_ref, ko_ref, vo_ref):
        hp = pl.program_id(1)
        cv = c_ref[...]
        v_extras = jnp.where(lax.broadcasted_iota(jnp.int32, (tm, HEAD_DIM), 1) < 3, 1.0, 0.0).astype(vo_ref.dtype)
        for hh in range(2):
            ch = _pieces(_head_column(cv, 2 * hp + hh))
            src = slice(hh * HEAD_DIM, (hh + 1) * HEAD_DIM)
            lo = slice(hh * PACK, hh * PACK + HEAD_DIM)
            hi = slice(hh * PACK + HEAD_DIM, (hh + 1) * PACK)
            qo_ref[:, lo] = (q_ref[:, src].astype(F32) * ATTN_SCALE).astype(qo_ref.dtype)
            qo_ref[:, hi] = _extras(ch, ONES3, tm).astype(qo_ref.dtype)
            ko_ref[:, lo] = k_ref[:, src]
            ko_ref[:, hi] = _extras(ONES3, [-p for p in ch], tm).astype(ko_ref.dtype)
            vo_ref[:, lo] = v_ref[:, src]
            vo_ref[:, hi] = v_extras

    def src(block0):
        return pl.BlockSpec((tm, PACK), lambda i, hp: (i, block0 + hp))

    out = pl.BlockSpec((tm, 2 * PACK), lambda i, hp: (i, hp))
    shp = jax.ShapeDtypeStruct((S, N_FOX_HEADS * PACK), _CD)
    return _pcall(body, name=name, grid=(S // tm, HEAD_PAIRS),
                  in_specs=[src(Q_BLOCK0), src(K_BLOCK0), src(V_BLOCK0), pl.BlockSpec((tm, PACK), lambda i, hp: (i, 0))],
                  out_specs=[out, out, out], out_shape=[shp, shp, shp],
                  compiler_params=_params("parallel", "parallel"))(qkv, qkv, qkv, c)


def _fox_fwd(qp, kp, vp, *, name):
    S = qp.shape[0]
    nt = S // FOX_T
    nt_dims = (((1,), (1,)), ((), ()))
    tn_dims = (((0,), (0,)), ((), ()))

    def body(i_tab, j_tab, q_ref, k_ref, v_ref, o_ref, l_ref, m_s, acc_s):
        t = pl.program_id(1)
        i, j = i_tab[t], j_tab[t]

        @pl.when(j == 0)
        def _():
            m_s[...] = jnp.full((FOX_HPS, 1, FOX_T), NEG_INF, F32)
            acc_s[...] = jnp.zeros((FOX_HPS, PACK, FOX_T), F32)

        def tile(diagonal):
            for hh in range(FOX_HPS):
                cols = slice(hh * PACK, (hh + 1) * PACK)
                st = lax.dot_general(k_ref[:, cols], q_ref[:, cols], nt_dims, preferred_element_type=F32)
                if diagonal:
                    key = lax.broadcasted_iota(jnp.int32, (FOX_T, FOX_T), 0)
                    qry = lax.broadcasted_iota(jnp.int32, (FOX_T, FOX_T), 1)
                    st = jnp.where(key <= qry, st, NEG_INF)
                m_old = m_s[hh]
                m_new = jnp.maximum(m_old, jnp.max(st, axis=0, keepdims=True))
                pt = jnp.exp(st - m_new)
                acc_s[hh] = jnp.exp(m_old - m_new) * acc_s[hh] + lax.dot_general(
                    v_ref[:, cols], pt.astype(_CD), tn_dims, preferred_element_type=F32)
                m_s[hh] = m_new

        @pl.when(j < i)
        def _():
            tile(False)

        @pl.when(j == i)
        def _():
            tile(True)
            for hh in range(FOX_HPS):
                acc = acc_s[hh]
                den = acc[HEAD_DIM:HEAD_DIM + 1, :]
                cols = slice(hh * HEAD_DIM, (hh + 1) * HEAD_DIM)
                o_ref[:, cols] = (acc[:HEAD_DIM, :] / den).T
                l_ref[:, cols] = jnp.broadcast_to(m_s[hh] + jnp.log(den), (HEAD_DIM, FOX_T)).T

    pairs = [(i, j) for i in range(nt) for j in range(i + 1)]
    i_tab = jnp.asarray([p[0] for p in pairs], jnp.int32)
    j_tab = jnp.asarray([p[1] for p in pairs], jnp.int32)
    qs = pl.BlockSpec((FOX_T, FOX_HPS * PACK), lambda hp, t, it, jt: (it[t], hp))
    ks = pl.BlockSpec((FOX_T, FOX_HPS * PACK), lambda hp, t, it, jt: (jt[t], hp))
    os_ = pl.BlockSpec((FOX_T, FOX_HPS * HEAD_DIM), lambda hp, t, it, jt: (it[t], hp))
    shp = jax.ShapeDtypeStruct((S, FOX_WIDTH), F32)
    grid_spec = pltpu.PrefetchScalarGridSpec(
        num_scalar_prefetch=2, grid=(N_FOX_HEADS // FOX_HPS, len(pairs)), in_specs=[qs, ks, ks], out_specs=[os_, os_],
        scratch_shapes=[pltpu.VMEM((FOX_HPS, 1, FOX_T), F32), pltpu.VMEM((FOX_HPS, PACK, FOX_T), F32)])
    return _pcall(body, name=name, grid_spec=grid_spec, out_shape=[shp, shp],
                  compiler_params=_params("parallel", "arbitrary"))(i_tab, j_tab, qp, kp, vp)


def _fox_pack_bwd(qkv, c, o, lse, do, *, name, tm=512, after=None):
    S = qkv.shape[0]

    def body(q_ref, c_ref, o_ref, l_ref, do_ref, qo_ref, do_out_ref):
        hp = pl.program_id(1)
        cv = c_ref[...]
        for hh in range(2):
            src = slice(hh * HEAD_DIM, (hh + 1) * HEAD_DIM)
            lo = slice(hh * PACK, hh * PACK + HEAD_DIM)
            hi = slice(hh * PACK + HEAD_DIM, (hh + 1) * PACK)
            shift = _head_column(cv, 2 * hp + hh) - l_ref[:, hh * HEAD_DIM:hh * HEAD_DIM + 1]
            dov = do_ref[:, src]
            dsum = jnp.sum(dov * o_ref[:, src], axis=-1, keepdims=True)
            qo_ref[:, lo] = (q_ref[:, src].astype(F32) * ATTN_SCALE).astype(qo_ref.dtype)
            qo_ref[:, hi] = _extras(_pieces(shift), ONES3, tm).astype(qo_ref.dtype)
            do_out_ref[:, lo] = dov.astype(do_out_ref.dtype)
            do_out_ref[:, hi] = _extras(_pieces(-dsum), ZEROS3, tm).astype(do_out_ref.dtype)

    pair = pl.BlockSpec((tm, PACK), lambda i, hp: (i, hp))
    out = pl.BlockSpec((tm, 2 * PACK), lambda i, hp: (i, hp))
    shp = jax.ShapeDtypeStruct((S, N_FOX_HEADS * PACK), _CD)
    return _pcall(body, after, name=name, grid=(S // tm, HEAD_PAIRS),
                  in_specs=[pl.BlockSpec((tm, PACK), lambda i, hp: (i, Q_BLOCK0 + hp)),
                            pl.BlockSpec((tm, PACK), lambda i, hp: (i, 0)), pair, pair, pair],
                  out_specs=[out, out], out_shape=[shp, shp],
                  compiler_params=_params("parallel", "parallel"))(qkv, c, o, lse, do)


def _fox_bwd(qp, kp, vp, dop, *, name):
    S = qp.shape[0]
    nt = S // FOX_T
    nt_dims = (((1,), (1,)), ((), ()))
    tn_dims = (((0,), (0,)), ((), ()))

    def body(i_tab, j_tab, q_ref, k_ref, v_ref, do_ref, dq_ref, dk_ref, dv_ref, dc_ref, dr_ref,
             dq_s, dk_s, dv_s, dc_s, dr_s):
        t = pl.program_id(1)
        i, j = i_tab[t], j_tab[t]

        @pl.when(t == 0)
        def _():
            dq_s[...] = jnp.zeros((S, FOX_HPS * PACK), F32)
            dr_s[...] = jnp.zeros((FOX_HPS, 1, S), F32)

        @pl.when(i == j)
        def _():
            dk_s[...] = jnp.zeros((FOX_T, FOX_HPS * PACK), F32)
            dv_s[...] = jnp.zeros((FOX_T, FOX_HPS * PACK), F32)
            dc_s[...] = jnp.zeros((FOX_HPS, FOX_T, 1), F32)

        def tile(diagonal):
            rows = pl.ds(pl.multiple_of(i * FOX_T, FOX_T), FOX_T)
            for hh in range(FOX_HPS):
                cols = slice(hh * PACK, (hh + 1) * PACK)
                qv, kv, vv, dov = q_ref[:, cols], k_ref[:, cols], v_ref[:, cols], do_ref[:, cols]
                pt = jnp.exp(lax.dot_general(kv, qv, nt_dims, preferred_element_type=F32))
                if diagonal:
                    key = lax.broadcasted_iota(jnp.int32, (FOX_T, FOX_T), 0)
                    qry = lax.broadcasted_iota(jnp.int32, (FOX_T, FOX_T), 1)
                    pt = jnp.where(key <= qry, pt, 0.0)
                dst = pt * lax.dot_general(vv, dov, nt_dims, preferred_element_type=F32)
                dsb = dst.astype(_CD)
                dc_s[hh] += jnp.sum(dst, axis=1, keepdims=True)
                dr_s[hh, :, rows] += jnp.sum(dst, axis=0, keepdims=True)
                dv_s[:, cols] += jnp.dot(pt.astype(_CD), dov, preferred_element_type=F32)
                dk_s[:, cols] += jnp.dot(dsb, qv, preferred_element_type=F32)
                dq_s[rows, cols] += lax.dot_general(dsb, kv, tn_dims, preferred_element_type=F32)

        @pl.when(i > j)
        def _():
            tile(False)

        @pl.when(i == j)
        def _():
            tile(True)

        @pl.when(i == nt - 1)
        def _():
            for hh in range(FOX_HPS):
                src = slice(hh * PACK, hh * PACK + HEAD_DIM)
                dst_cols = slice(hh * HEAD_DIM, (hh + 1) * HEAD_DIM)
                dk_ref[:, dst_cols] = dk_s[:, src].astype(dk_ref.dtype)
                dv_ref[:, dst_cols] = dv_s[:, src].astype(dv_ref.dtype)
                dc_ref[:, dst_cols] = jnp.broadcast_to(dc_s[hh], (FOX_T, HEAD_DIM))

        @pl.when(t == len(pairs) - 1)
        def _():
            for hh in range(FOX_HPS):
                dq_ref[:, hh * HEAD_DIM:(hh + 1) * HEAD_DIM] = (
                    dq_s[:, hh * PACK:hh * PACK + HEAD_DIM] * ATTN_SCALE).astype(dq_ref.dtype)
            dr_ref[...] = dr_s[...]

    pairs = [(i, j) for j in range(nt) for i in range(j, nt)]
    i_tab = jnp.asarray([p[0] for p in pairs], jnp.int32)
    j_tab = jnp.asarray([p[1] for p in pairs], jnp.int32)
    wide, narrow = FOX_HPS * PACK, FOX_HPS * HEAD_DIM
    qs = pl.BlockSpec((FOX_T, wide), lambda hp, t, it, jt: (it[t], hp))
    ks = pl.BlockSpec((FOX_T, wide), lambda hp, t, it, jt: (jt[t], hp))
    whole = pl.BlockSpec((S, narrow), lambda hp, t, it, jt: (0, hp))
    cs = pl.BlockSpec((FOX_T, narrow), lambda hp, t, it, jt: (jt[t], hp))
    rs = pl.BlockSpec((FOX_HPS, 1, S), lambda hp, t, it, jt: (hp, 0, 0))
    shp = jax.ShapeDtypeStruct((S, FOX_WIDTH), _CD)
    grid_spec = pltpu.PrefetchScalarGridSpec(
        num_scalar_prefetch=2, grid=(N_FOX_HEADS // FOX_HPS, len(pairs)), in_specs=[qs, ks, ks, qs],
        out_specs=[whole, cs, cs, cs, rs],
        scratch_shapes=[pltpu.VMEM((S, wide), F32), pltpu.VMEM((FOX_T, wide), F32),
                        pltpu.VMEM((FOX_T, wide), F32), pltpu.VMEM((FOX_HPS, FOX_T, 1), F32),
                        pltpu.VMEM((FOX_HPS, 1, S), F32)])
    return _pcall(body, name=name, grid_spec=grid_spec,
                  out_shape=[shp, shp, shp, jax.ShapeDtypeStruct((S, FOX_WIDTH), F32),
                             jax.ShapeDtypeStruct((N_FOX_HEADS, 1, S), F32)],
                  compiler_params=_params("parallel", "arbitrary"))(i_tab, j_tab, qp, kp, vp, dop)


def _layer_step(x, tgt, w, p, late_weights=None, grad_sink=None, after=None, first_weights=None):
    S = x.shape[0]
    after_norm, after_proj = after if after is not None else (None, None)
    h = _rms_fwd(x, p["norm_mix_g"], name="rms_mix", after=after_norm)
    if first_weights is not None:
        w = {**w, **first_weights(h)}
    qkv = _mm(h, w["qkv"][:, 3 * DIL_WIDTH:], name="proj_fox", out_dtype=_CD, tn=768, tm=2048, after=after_proj)
    dil_qkv = _proj_dil(h, w["qkv"], name="proj_dil")
    zf = _mm(h, w["f"], name="proj_f")
    gl = _mm(h, w["g"], name="proj_gate", tn=1024, out_dtype=_CD)

    dil_o, dil_l = [], []
    for g in range(N_DIL_GROUPS):
        og, lg = _dil_fwd(dil_qkv[g], g, name=f"dil_fwd{g}")
        dil_o.append(og), dil_l.append(lg)
    o_a = _dil_mix_fwd(dil_o, dil_l, name="dil_mix")

    c = _fox_cumsum(zf, p["b_fgt"], name="fox_cumsum")
    fqp, fkp, fvp = _fox_pack_fwd(qkv, c, name="fox_pack")
    o_b, flse = _fox_fwd(fqp, fkp, fvp, name="fox_fwd")

    if late_weights is not None:
        w = {**w, **late_weights(o_b)}
    y_a = _mm(o_a, w["dil_out"], name="y_a", tn=1024, out_dtype=_CD)
    y_b = _mm(o_b, w["fox_out"], name="y_b", tn=1024, out_dtype=_CD)
    merged, x1, h2 = _gated_mix_out(gl, p["b_gate"], y_a, y_b, w["out"], x, p["norm_ffn_g"], name="mix_out")
    gate, up, act = _ffn_in_act(h2, w["ffn_in"], name="ffn_in")
    loss, dx2, dg_final = _ffn_down_loss(act, w["ffn_down"], x1, p["norm_final_g"], tgt, name="ffn_down_loss")

    gw_ffn_down = _mm(act, dx2, name="gw_ffn_down", ta=True, out_dtype=_CD, tm=1408)
    dgu = _d_swiglu(dx2, w["ffn_down"], gate, up, name="d_swiglu")
    gw_ffn_in = _mm(h2, dgu, name="gw_ffn_in", ta=True, out_dtype=_CD, tn=1408, out_blocks=1408, b_halves=True)
    sink = grad_sink if grad_sink is not None else (lambda group, grads: None)
    tok = sink("ffn", dict(ffn_in=gw_ffn_in, ffn_down=gw_ffn_down))
    dx1, dg_ffn = _mm(dgu, w["ffn_in"], name="d_h2", tb=True, tk=1408, b_blocks=True, tm=1024, a_halves=True,
                      rms_bwd=(x1, p["norm_ffn_g"], dx2), after=tok)

    dmerged = _mm(dx1, w["out"], name="d_merged", tb=True, out_dtype=_CD)
    gw_out = _mm(merged, dx1, name="gw_out", ta=True, out_dtype=_CD)
    dy_a, dy_b, dgl, db_gate = _gate_bwd(dmerged, gl, p["b_gate"], y_a, y_b, name="gate_bwd")
    do_a = _mm(dy_a, w["dil_out"], name="d_o_a", tb=True)
    gw_dil_out = _mm(o_a, dy_a, name="gw_dil_out", ta=True, out_dtype=_CD, tn=1024)
    do_b = _mm(dy_b, w["fox_out"], name="d_o_b", tb=True)
    gw_fox_out = _mm(o_b, dy_b, name="gw_fox_out", ta=True, out_dtype=_CD, tn=1024)
    tok = sink("mix", dict(dil_out=gw_dil_out, fox_out=gw_fox_out, out=gw_out))

    bqp, bdop = _fox_pack_bwd(qkv, c, o_b, flse, do_b, name="fox_pack_bwd", after=tok)
    dqp, dkp, dvp, dck, dcq = _fox_bwd(bqp, fkp, fvp, bdop, name="fox_bwd")
    dc = dcq[:, 0, :].T - dck.reshape(S, N_FOX_HEADS, HEAD_DIM)[:, :, 0]
    dc = jnp.pad(dc, ((0, 0), (0, F_PAD - N_FOX_HEADS)))
    dzf, db_fgt = _fox_cumsum_bwd(dc, zf, p["b_fgt"], name="fox_cumsum_bwd")

    douts = _dil_mix_bwd(do_a, dil_o, dil_l, name="dil_mix_bwd", after=tok)
    dqs, dks, dvs = [], [], []
    for g in range(N_DIL_GROUPS):
        dq, dk, dv = _dil_bwd(dil_qkv[g], douts[3 + g], douts[g], g, name=f"dil_bwd{g}")
        for parts, t in ((dqs, dq), (dks, dk), (dvs, dv)):
            parts.extend([t[0].astype(_CD), t[1].astype(_CD)])
    dqkv = jnp.concatenate(dqs + dks + dvs + [dqp, dkp, dvp], axis=1)

    gw_qkv = _mm(h, dqkv, name="gw_qkv", ta=True, out_dtype=_CD, tn=768)
    gw_g = _mm(h, dgl, name="gw_gate", ta=True, out_dtype=_CD)
    gw_f = _mm(h, dzf, name="gw_f", ta=True, out_dtype=_CD)
    tok = sink("in", dict(qkv=gw_qkv, f=gw_f, g=gw_g))
    dh = _mm(dqkv, w["qkv"], name="d_h_qkv", tb=True, tk=1920, tm=2048, after=tok)
    dh = _mm(dgl, w["g"], name="d_h_gate", tb=True, add=dh)
    dx, dg_mix = _mm(dzf, w["f"], name="d_h_f", tb=True, add=dh, tm=512, rms_bwd=(x, p["norm_mix_g"], dx1))

    gw = dict(qkv=gw_qkv, f=gw_f, g=gw_g, dil_out=gw_dil_out, fox_out=gw_fox_out, out=gw_out, ffn_in=gw_ffn_in,
              ffn_down=gw_ffn_down)
    small = dict(norm_mix_g=dg_mix, b_fgt=db_fgt, b_gate=db_gate, norm_ffn_g=dg_ffn, norm_final_g=dg_final)
    return loss, dx, gw, small


def _position():
    return lax.axis_index("x"), lax.axis_index("y"), lax.axis_index("c")


def _other_chips(x, y):
    return [(1 - x, y), (x, 1 - y), (1 - x, 1 - y)]


ROW_TILE = 16


def _row_chunks(rows, want=4):
    n = want
    while n > 1 and rows % (n * ROW_TILE):
        n //= 2
    return n


SEM_SPEC = pl.BlockSpec(memory_space=pltpu.SEMAPHORE)
ANY_SPEC = pl.BlockSpec(memory_space=pl.ANY)
DATAFLOW = pltpu.SideEffectType.DATAFLOW_SIDE_EFFECTING


def _in_hbm(a):
    return pltpu.with_memory_space_constraint(a, pltpu.HBM)


def _split_copy_start(srcs, land_shapes, copies, after, *, name):
    n, m = len(srcs), len(land_shapes)

    def body(*refs):
        src_refs, land_refs = refs[:n], refs[n:n + m]
        send_sems, recv_sems = refs[n + m + 1], refs[n + m + 2]
        token = refs[-1]
        x, y, c = _position()
        for k, (src, dst, peer) in enumerate(copies(x, y, c, src_refs, land_refs)):
            pltpu.make_async_remote_copy(src_ref=src, dst_ref=dst, send_sem=send_sems.at[k], recv_sem=recv_sems.at[k],
                                         device_id=peer, device_id_type=MESH).start()
        token[...] = jnp.zeros_like(token)

    lands = [lax.empty(s.shape, s.dtype) for s in land_shapes]
    count = len(copies(0, 0, 0, srcs, lands))
    out = _pcall(
        body, name=name,
        out_shape=(pltpu.SemaphoreType.DMA((count,)), pltpu.SemaphoreType.DMA((count,)),
                   *[pltpu.HBM(s.shape, s.dtype) for s in srcs], *[pltpu.HBM(s.shape, s.dtype) for s in land_shapes],
                   jax.ShapeDtypeStruct((8, 128), F32)),
        in_specs=[HBM_SPEC] * (n + m) + [ANY_SPEC],
        out_specs=(SEM_SPEC, SEM_SPEC, *[HBM_SPEC] * (n + m), pl.BlockSpec(memory_space=pltpu.VMEM)),
        input_output_aliases={k: 2 + k for k in range(n + m)},
        compiler_params=pltpu.CompilerParams(has_side_effects=DATAFLOW),
    )(*[_in_hbm(s) for s in srcs], *[_in_hbm(l) for l in lands], after)
    return out[0], out[1], list(out[2:2 + n]), list(out[2 + n:2 + n + m]), out[-1]


def _split_copy_wait(send_sems, recv_sems, srcs, lands, copies, after, *, name):
    n, m = len(srcs), len(lands)

    def body(*refs):
        src_refs, land_refs = refs[:n], refs[n:n + m]
        send, recv = refs[n + m], refs[n + m + 1]
        x, y, c = _position()
        for k, (src, dst, peer) in enumerate(copies(x, y, c, src_refs, land_refs)):
            cp = pltpu.make_async_remote_copy(src_ref=src, dst_ref=dst, send_sem=send.at[k], recv_sem=recv.at[k],
                                              device_id=peer, device_id_type=MESH)
            cp.wait_send()
            cp.wait_recv()

    afters = list(after) if isinstance(after, (list, tuple)) else [after]
    out = _pcall(
        body, name=name,
        out_shape=tuple(pltpu.HBM(s.shape, s.dtype) for s in list(srcs) + list(lands)),
        in_specs=[HBM_SPEC] * (n + m) + [SEM_SPEC, SEM_SPEC] + [ANY_SPEC] * len(afters),
        out_specs=tuple([HBM_SPEC] * (n + m)),
        input_output_aliases={k: k for k in range(n + m)},
        compiler_params=pltpu.CompilerParams(has_side_effects=DATAFLOW),
    )(*srcs, *lands, send_sems, recv_sems, *afters)
    return list(out[:n]), list(out[n:])


def _gather_copies(x, y, c, shard_refs, land_refs):
    out = []
    for s, l in zip(shard_refs, land_refs):
        half = s.shape[0] // 2
        nq = _row_chunks(half)
        for cx, cy in _other_chips(x, y):
            for q in range(nq):
                rows = pl.ds(c * half + q * (half // nq), half // nq)
                out.append((s.at[rows, :], l.at[2 * x + y, rows, :], (cx, cy, c)))
    return out


def _gather_whole_copies(x, y, c, shard_refs, land_refs):
    out = []
    for s, l in zip(shard_refs, land_refs):
        nq = _row_chunks(s.shape[0])
        for cx, cy in _other_chips(x, y):
            for q in range(nq):
                rows = pl.ds(q * (s.shape[0] // nq), s.shape[0] // nq)
                out.append((s.at[rows, :], l.at[2 * x + y, rows, :], (cx, cy, c)))
    return out


def _scatter_all_copies(x, y, c, block_refs, land_refs):
    out = []
    for g, l in zip(block_refs, land_refs):
        half = g.shape[1] // 2
        nq = _row_chunks(half)
        size = half // nq
        for q in range(nq):
            rows = pl.ds((1 - c) * half + q * size, size)
            out.append((g.at[2 * x + y, rows, :], l.at[0, pl.ds(q * size, size), :], (x, y, 1 - c)))
        for r, (cx, cy) in enumerate(_other_chips(x, y)):
            for j in range(2):
                h = c if j == 0 else 1 - c
                for q in range(nq):
                    rows = pl.ds(h * half + q * size, size)
                    out.append((g.at[2 * cx + cy, rows, :], l.at[1 + 2 * r + j, pl.ds(q * size, size), :], (cx, cy, h)))
    return out


def _forward_halves(lands, *, name):
    n = len(lands)

    def body(*refs):
        ins = refs[:n]
        send_sems, recv_sems = refs[2 * n:]
        x, y, c = _position()
        copies = []
        for w in range(n):
            half = ins[w].shape[1] // 2
            for r, (cx, cy) in enumerate(_other_chips(x, y)):
                blk = ins[w].at[2 * cx + cy, pl.ds(c * half, half), :]
                cp = pltpu.make_async_remote_copy(src_ref=blk, dst_ref=blk, send_sem=send_sems.at[w, r],
                                                  recv_sem=recv_sems.at[w, r], device_id=(x, y, 1 - c),
                                                  device_id_type=MESH)
                cp.start()
                copies.append(cp)
        for w in range(n):
            half = ins[w].shape[1] // 2
            for r, (cx, cy) in enumerate(_other_chips(x, y)):
                blk = ins[w].at[2 * cx + cy, pl.ds((1 - c) * half, half), :]
                pltpu.make_async_remote_copy(src_ref=blk, dst_ref=blk, send_sem=send_sems.at[w, r],
                                             recv_sem=recv_sems.at[w, r], device_id=(x, y, 1 - c),
                                             device_id_type=MESH).wait_recv()
        for cp in copies:
            cp.wait_send()

    return _pcall(
        body, name=name, in_specs=[HBM_SPEC] * n, out_specs=[HBM_SPEC] * n,
        out_shape=[jax.ShapeDtypeStruct(l.shape, l.dtype) for l in lands],
        input_output_aliases={k: k for k in range(n)},
        scratch_shapes=[pltpu.SemaphoreType.DMA((n, 3)), pltpu.SemaphoreType.DMA((n, 3))],
    )(*lands)


def _share_halves(halves):
    n = len(halves)

    def body(*refs):
        ins, outs = refs[:n], refs[n:2 * n]
        send_sems, recv_sems = refs[2 * n:]
        x, y, c = _position()
        copies = []
        for w in range(n):
            cp = pltpu.make_async_remote_copy(src_ref=ins[w], dst_ref=outs[w], send_sem=send_sems.at[w],
                                              recv_sem=recv_sems.at[w], device_id=(x, y, 1 - c), device_id_type=MESH)
            cp.start()
            copies.append(cp)
        for cp in copies:
            cp.wait()

    return _pcall(
        body, name="share_halves", in_specs=[HBM_SPEC] * n, out_specs=[HBM_SPEC] * n,
        out_shape=[jax.ShapeDtypeStruct(h.shape, h.dtype) for h in halves],
        scratch_shapes=[pltpu.SemaphoreType.DMA((n,)), pltpu.SemaphoreType.DMA((n,))],
    )(*halves)


def _sum_small(part):
    rows, width = part.shape

    def body(x_ref, out_ref, all_ref, send_sems, recv_sems):
        x, y, c = _position()
        me, sibling = (x, y, c), (x, y, 1 - c)
        chips = _other_chips(x, y)

        def block(px, py, pc):
            return all_ref.at[pl.ds((4 * px + 2 * py + pc) * rows, rows), :]

        def copy(k, blk, to, src=None):
            return pltpu.make_async_remote_copy(
                src_ref=block(*blk) if src is None else src, dst_ref=block(*blk), send_sem=send_sems.at[k],
                recv_sem=recv_sems.at[k], device_id=to, device_id_type=MESH)

        all_ref[pl.ds((4 * x + 2 * y + c) * rows, rows), :] = x_ref[...]
        first = [copy(0, me, sibling, src=x_ref)]
        first += [copy(1 + j, me, (*chip, c), src=x_ref) for j, chip in enumerate(chips)]
        for cp in first:
            cp.start()
        passed = [copy(4 + j, (*chip, c), sibling) for j, chip in enumerate(chips)]
        for j, chip in enumerate(chips):
            copy(1 + j, (*chip, c), me).wait_recv()
            passed[j].start()
        copy(0, sibling, me).wait_recv()
        for j, chip in enumerate(chips):
            copy(4 + j, (*chip, 1 - c), me).wait_recv()
        for cp in first + passed:
            cp.wait_send()
        total = all_ref[0:rows, :]
        for d in range(1, 8):
            total = total + all_ref[d * rows:(d + 1) * rows, :]
        out_ref[...] = total

    vm = pl.BlockSpec(memory_space=pltpu.VMEM)
    return _pcall(
        body, name="sum_small", in_specs=[vm], out_specs=vm, out_shape=jax.ShapeDtypeStruct((rows, width), F32),
        scratch_shapes=[pltpu.VMEM((8 * rows, width), F32), pltpu.SemaphoreType.DMA((7,)), pltpu.SemaphoreType.DMA((7,))],
    )(part)


def _row_tile(R, C, itemsize=4, budget=1 << 20):
    for t in (512, 256, 128, 64, 32, 16, 8):
        if R % t == 0 and t * C * itemsize <= budget:
            return t
    return R


def _add_all(g, recv, where, *, name):
    _, R, C = g.shape
    half = R // 2
    t = _row_tile(half, C)
    nb = half // t

    def body(w_ref, g_ref, r_ref, o_ref):
        total = g_ref[0].astype(F32)
        for k in range(7):
            total = total + r_ref[k].astype(F32)
        o_ref[...] = total

    grid_spec = pltpu.PrefetchScalarGridSpec(
        num_scalar_prefetch=1, grid=(nb,),
        in_specs=[pl.BlockSpec((1, t, C), lambda i, wr: (wr[0], wr[1] * nb + i, 0)),
                  pl.BlockSpec((7, t, C), lambda i, wr: (0, i, 0))],
        out_specs=pl.BlockSpec((t, C), lambda i, wr: (i, 0)))
    return _pcall(body, name=name, grid_spec=grid_spec, out_shape=jax.ShapeDtypeStruct((half, C), F32),
                  compiler_params=_params("parallel"))(where, g, recv)


def _adamw(w, g, m, v, *, name):
    R, C = w.shape
    t = _row_tile(R, C)
    c1 = 1.0 - ADAM_B1 ** ADAM_STEP
    c2 = 1.0 - ADAM_B2 ** ADAM_STEP

    def body(w_ref, g_ref, m_ref, v_ref, d_ref, nm_ref, nv_ref):
        gv = g_ref[...]
        mn = ADAM_B1 * m_ref[...] + (1.0 - ADAM_B1) * gv
        vn = ADAM_B2 * v_ref[...] + (1.0 - ADAM_B2) * (gv * gv)
        d_ref[...] = -ADAM_LR * ((mn / c1) / (jnp.sqrt(vn / c2) + ADAM_EPS) + ADAM_WD * w_ref[...])
        nm_ref[...] = mn
        nv_ref[...] = vn

    blk = pl.BlockSpec((t, C), lambda i: (i, 0))
    shp = jax.ShapeDtypeStruct((R, C), F32)
    return _pcall(body, name=name, grid=(R // t,), in_specs=[blk] * 4, out_specs=[blk] * 3, out_shape=[shp] * 3,
                  compiler_params=_params("parallel"))(w, g, m, v)


def _adamw_halves(w, mine, theirs, m, v, core, *, name):
    R, C = w.shape
    half = R // 2
    t = _row_tile(half, C)
    nbh = half // t
    c1 = 1.0 - ADAM_B1 ** ADAM_STEP
    c2 = 1.0 - ADAM_B2 ** ADAM_STEP

    def body(core_ref, w_ref, a_ref, b_ref, m_ref, v_ref, g_ref, d_ref, nm_ref, nv_ref):
        gv = jnp.where(pl.program_id(0) // nbh == core_ref[0], a_ref[...], b_ref[...])
        mn = ADAM_B1 * m_ref[...] + (1.0 - ADAM_B1) * gv
        vn = ADAM_B2 * v_ref[...] + (1.0 - ADAM_B2) * (gv * gv)
        g_ref[...] = gv
        d_ref[...] = -ADAM_LR * ((mn / c1) / (jnp.sqrt(vn / c2) + ADAM_EPS) + ADAM_WD * w_ref[...])
        nm_ref[...] = mn
        nv_ref[...] = vn

    blk = pl.BlockSpec((t, C), lambda i, cr: (i, 0))
    hblk = pl.BlockSpec((t, C), lambda i, cr: (i % nbh, 0))
    shp = jax.ShapeDtypeStruct((R, C), F32)
    grid_spec = pltpu.PrefetchScalarGridSpec(num_scalar_prefetch=1, grid=(2 * nbh,),
                                             in_specs=[blk, hblk, hblk, blk, blk], out_specs=[blk] * 4)
    return _pcall(body, name=name, grid_spec=grid_spec, out_shape=[shp] * 4,
                  compiler_params=_params("parallel"))(core, w, mine, theirs, m, v)


BIG = ("w_in", "w_dil_out", "w_fox_out", "w_out", "w_ffn_in", "w_ffn_down")
SMALL = ("norm_mix_g", "b_fgt", "b_gate", "norm_ffn_g", "norm_final_g")
ORDER = ("norm_mix_g", "w_in", "b_fgt", "b_gate", "w_dil_out", "w_fox_out", "w_out", "norm_ffn_g", "w_ffn_in",
         "w_ffn_down", "norm_final_g")
SMALL_ROWS = {"norm_mix_g": (0, 1), "b_gate": (1, 3), "norm_ffn_g": (3, 4), "norm_final_g": (4, 5), "b_fgt": (5, 6)}


def _columns_to_blocks(full, ncol):
    K = full.shape[0]
    return full.reshape(K, 4, ncol).transpose(1, 0, 2)


def _blocks_to_columns(blocks):
    n, K, ncol = blocks.shape
    return blocks.transpose(1, 0, 2).reshape(K, n * ncol)


def kernel(x, norm_mix_g, w_in, b_fgt, b_gate, w_dil_out, w_fox_out, w_out, norm_ffn_g, w_ffn_in, w_ffn_down, norm_final_g, loss_target, m_norm_mix_g, m_w_in, m_b_fgt, m_b_gate, m_w_dil_out, m_w_fox_out, m_w_out, m_norm_ffn_g, m_w_ffn_in, m_w_ffn_down, m_norm_final_g, v_norm_mix_g, v_w_in, v_b_fgt, v_b_gate, v_w_dil_out, v_w_fox_out, v_w_out, v_norm_ffn_g, v_w_ffn_in, v_w_ffn_down, v_norm_final_g):
    weights = dict(norm_mix_g=norm_mix_g, w_in=w_in, b_fgt=b_fgt, b_gate=b_gate, w_dil_out=w_dil_out,
                   w_fox_out=w_fox_out, w_out=w_out, norm_ffn_g=norm_ffn_g, w_ffn_in=w_ffn_in, w_ffn_down=w_ffn_down,
                   norm_final_g=norm_final_g)
    m_in = dict(norm_mix_g=m_norm_mix_g, w_in=m_w_in, b_fgt=m_b_fgt, b_gate=m_b_gate, w_dil_out=m_w_dil_out,
                w_fox_out=m_w_fox_out, w_out=m_w_out, norm_ffn_g=m_norm_ffn_g, w_ffn_in=m_w_ffn_in,
                w_ffn_down=m_w_ffn_down, norm_final_g=m_norm_final_g)
    v_in = dict(norm_mix_g=v_norm_mix_g, w_in=v_w_in, b_fgt=v_b_fgt, b_gate=v_b_gate, w_dil_out=v_w_dil_out,
                w_fox_out=v_w_fox_out, w_out=v_w_out, norm_ffn_g=v_norm_ffn_g, w_ffn_in=v_w_ffn_in,
                w_ffn_down=v_w_ffn_down, norm_final_g=v_norm_final_g)
    c = lax.axis_index("c")
    chip = 2 * lax.axis_index("x") + lax.axis_index("y")

    shards = {n: weights[n][0].astype(_CD) for n in BIG}
    in_shape = jax.ShapeDtypeStruct((4,) + shards["w_in"].shape, _CD)
    send_i, recv_i, in_src, in_land, token_in = _split_copy_start(
        [shards["w_in"]], [in_shape], _gather_copies, norm_mix_g, name="gather_in_start")
    late = BIG[1:]
    send_g, recv_g, late_src, late_land, token = _split_copy_start(
        [shards[n] for n in late], [jax.ShapeDtypeStruct((4,) + shards[n].shape, _CD) for n in late],
        _gather_whole_copies, token_in, name="gather_late_start")
    adam_in = [t[0] + token_in[0, 0] for t in (w_in, m_w_in, v_w_in)]
    p = dict(norm_mix_g=norm_mix_g, b_fgt=jnp.pad(b_fgt, ((0, 0), (0, F_PAD - N_FOX_HEADS))), b_gate=b_gate,
             norm_ffn_g=norm_ffn_g, norm_final_g=norm_final_g.reshape(1, D_MODEL))

    def first_weights(after):
        own, lands = _split_copy_wait(send_i, recv_i, in_src, in_land, _gather_copies, [after] + adam_in,
                                      name="gather_in_wait")
        (g_in,) = _forward_halves(lands, name="gather_in_forward")
        full_in = _blocks_to_columns(lax.dynamic_update_index_in_dim(g_in, own[0], chip, 0))
        o3 = QKV_COLS
        o4 = o3 + N_FOX_HEADS
        return dict(qkv=full_in[:, :o3], f=jnp.pad(full_in[:, o3:o4], ((0, 0), (0, F_PAD - N_FOX_HEADS))),
                    g=full_in[:, o4:])

    def late_weights(after):
        own, lands = _split_copy_wait(send_g, recv_g, late_src, late_land, _gather_whole_copies, after,
                                      name="gather_late_wait")
        g_dil, g_fox, g_out, g_ffn_in, g_ffn_down = [
            lax.dynamic_update_index_in_dim(l, s, chip, 0) for l, s in zip(lands, own)]
        return dict(dil_out=_blocks_to_columns(g_dil), fox_out=_blocks_to_columns(g_fox),
                    out=g_out.reshape(D_MODEL, D_MODEL), ffn_in=g_ffn_in,
                    ffn_down=g_ffn_down.reshape(D_FF, D_MODEL))

    def to_blocks(n, full):
        shape = weights[n].shape
        if full.ndim == 3:
            return full
        if n in ("w_out", "w_ffn_down"):
            return full.reshape(4, shape[1], shape[2])
        return _columns_to_blocks(full, shape[2])

    in_flight = {}

    def grad_sink(group, gw):
        if group == "in":
            named = {"w_in": jnp.concatenate([gw["qkv"], gw["f"][:, :N_FOX_HEADS], gw["g"]], axis=1)}
        else:
            named = {"w_" + k: v for k, v in gw.items()}
        srcs = [to_blocks(n, named[n]) for n in named]
        lands = [jax.ShapeDtypeStruct((7, s.shape[1] // 2, s.shape[2]), s.dtype) for s in srcs]
        started = _split_copy_start(srcs, lands, _scatter_all_copies, next(iter(gw.values())),
                                    name=f"scatter_{group}_start")
        in_flight[group] = (list(named), started)
        return started[-1]

    loss_part, grad_x, gw, small = _layer_step(x[0], loss_target[0], {}, p, late_weights, grad_sink,
                                               (token_in, token), first_weights)

    halves = {}
    where = jnp.stack([chip, c]).astype(jnp.int32)
    for group, (names, (send_s, recv_s, srcs, lands, _)) in in_flight.items():
        srcs, recv = _split_copy_wait(send_s, recv_s, srcs, lands, _scatter_all_copies, grad_x,
                                      name=f"scatter_{group}_wait")
        halves.update({n: _add_all(s, r, where, name=f"add_all_{n}") for n, s, r in zip(names, srcs, recv)})
    halves = [halves[n] for n in BIG]
    grad_halves = dict(zip(BIG, zip(halves, _share_halves(halves))))

    packed = jnp.concatenate([
        small["norm_mix_g"], small["b_gate"].reshape(2, D_MODEL), small["norm_ffn_g"], small["norm_final_g"],
        jnp.pad(small["b_fgt"], ((0, 0), (0, D_MODEL - F_PAD))), jnp.pad(loss_part, ((0, 0), (0, D_MODEL - 1))),
        jnp.zeros((1, D_MODEL), F32)], axis=0)
    summed = _sum_small(packed)
    loss = summed[6, 0]

    out_g, out_d, out_m, out_v = {}, {}, {}, {}
    core = jnp.reshape(c, (1,)).astype(jnp.int32)
    for n in SMALL:
        lo, hi = SMALL_ROWS[n]
        shape = weights[n].shape
        g2 = summed[lo:hi].reshape(1, -1)[:, :weights[n].size]
        d2, m2, v2 = _adamw(weights[n].reshape(g2.shape), g2, m_in[n].reshape(g2.shape), v_in[n].reshape(g2.shape),
                            name=f"adamw_{n}")
        out_g[n], out_d[n], out_m[n], out_v[n] = [t.reshape(shape) for t in (g2, d2, m2, v2)]
    for n in BIG:
        shape = weights[n].shape
        wmv = adam_in if n == "w_in" else [t[0] for t in (weights[n], m_in[n], v_in[n])]
        mine, theirs = grad_halves[n]
        outs = _adamw_halves(wmv[0], mine, theirs, wmv[1], wmv[2], core, name=f"adamw_{n}")
        out_g[n], out_d[n], out_m[n], out_v[n] = [t.reshape(shape) for t in outs]
    return (loss, grad_x[None], *[out_g[n] for n in ORDER], *[out_d[n] for n in ORDER],
            *[out_m[n] for n in ORDER], *[out_v[n] for n in ORDER])
```

```python
import numpy as np
import jax
import jax.numpy as jnp
from jax import lax
from jax.experimental import pallas as pl
from jax.experimental.pallas import tpu as pltpu

F32 = jnp.float32
_CD = jnp.bfloat16

D_MODEL = 1024
HEAD_DIM = 64
DIL_PAIRS = ((128, 1), (512, 4), (2048, 16))
N_DIL_GROUPS = 3
DIL_HEADS = 4
DIL_W = 128
DIL_OUT = DIL_HEADS * HEAD_DIM
DIL_WIDTH = N_DIL_GROUPS * DIL_OUT
N_FOX_HEADS = 8
FOX_WIDTH = N_FOX_HEADS * HEAD_DIM
D_FF = 2816
QKV_COLS = 3 * DIL_WIDTH + 3 * FOX_WIDTH
F_PAD = 128
RMS_EPS = 1e-6
NEG_INF = -1e30
ATTN_SCALE = HEAD_DIM ** -0.5
ADAM_LR, ADAM_B1, ADAM_B2, ADAM_EPS, ADAM_WD, ADAM_STEP = 0.001, 0.9, 0.999, 1e-08, 0.01, 10

VMEM_LIMIT = 48 * 1024 * 1024
VMEM_LIMIT_RESIDENT = 56 * 1024 * 1024
LANES = 128
MESH = pl.DeviceIdType.MESH
HBM_SPEC = pl.BlockSpec(memory_space=pltpu.HBM)


def _pcall(body, after=None, **kw):
    if after is None:
        return pl.pallas_call(body, **kw)
    n_in = len(kw["in_specs"])
    kw["in_specs"] = list(kw["in_specs"]) + [pl.BlockSpec(memory_space=pl.ANY)]

    def tied(*refs):
        return body(*refs[:n_in], *refs[n_in + 1:])

    call = pl.pallas_call(tied, **kw)
    return lambda *args: call(*args, after)


def _params(*sem):
    return pltpu.CompilerParams(dimension_semantics=sem, vmem_limit_bytes=VMEM_LIMIT)


def _pick(dim, pref):
    t = (min(pref, dim) // 128) * 128
    while t >= 128:
        if dim % t == 0:
            return t
        t -= 128
    return dim


def _mm(a, b, *, name, ta=False, tb=False, out_dtype=F32, add=None, tm=1024, tn=512, tk=2048, after=None,
        b_blocks=False, out_blocks=None, a_halves=False, b_halves=False, rms_bwd=None):
    if a_halves:
        M, K = a.shape[1], 2 * a.shape[2]
    elif ta:
        K, M = a.shape
    else:
        M, K = a.shape
    if b_halves:
        b_rows, b_cols = b.shape[1], 2 * b.shape[2]
    else:
        b_rows, b_cols = (b.shape[1], b.shape[0] * b.shape[2]) if b_blocks else b.shape
    if tb:
        N, K2 = b_rows, b_cols
    else:
        K2, N = b_rows, b_cols
    assert K == K2, (a.shape, b.shape)
    shard = b.shape[2] if b_blocks else None
    tm = _pick(M, tm)
    tn = _pick(shard if (b_blocks and not tb) else (out_blocks or N), tn)
    tk = _pick(shard if (b_blocks and tb) else K, tk)
    nk = K // tk
    dn = (((0 if ta else 1,), (1 if tb else 0,)), ((), ()))
    has_add = add is not None
    assert not (has_add and out_blocks)
    has_norm = rms_bwd is not None
    if has_norm:
        tn = N
        assert not out_blocks and out_dtype == F32

    def body(*refs):
        a_ref, b_ref = refs[0], refs[1]
        rest = list(refs[2:])
        add_ref = rest.pop(0) if has_add else None
        x_ref, g_ref, dres_ref = (rest.pop(0), rest.pop(0), rest.pop(0)) if has_norm else (None, None, None)
        o_ref = rest.pop(0)
        dg_ref = rest.pop(0) if has_norm else None
        bv = b_ref[0] if b_blocks else b_ref[...]
        p = lax.dot_general(a_ref[...].astype(_CD), bv.astype(_CD), dn, preferred_element_type=F32)

        def finish(r):
            if has_add:
                r = r + add_ref[...]
            if has_norm:
                xv = x_ref[...]
                rs = lax.rsqrt(jnp.mean(xv * xv, axis=-1, keepdims=True) + RMS_EPS)
                xh = xv * rs
                dxh = r * g_ref[...]
                o_ref[...] = dres_ref[...] + rs * (dxh - xh * jnp.mean(dxh * xh, axis=-1, keepdims=True))
                part = jnp.sum(r * xh, axis=0, keepdims=True)
                first = pl.program_id(0) == 0

                @pl.when(first)
                def _():
                    dg_ref[...] = part

                @pl.when(jnp.logical_not(first))
                def _():
                    dg_ref[...] += part
            elif out_blocks:
                o_ref[0] = r.astype(out_dtype)
            else:
                o_ref[...] = r.astype(out_dtype)

        if nk == 1:
            finish(p)
        else:
            acc_ref = rest.pop(0)
            k = pl.program_id(2)

            @pl.when(k == 0)
            def _():
                acc_ref[...] = p

            @pl.when(k > 0)
            def _():
                acc_ref[...] += p

            @pl.when(k == nk - 1)
            def _():
                finish(acc_ref[...])

    if a_halves:
        ka = (K // 2) // tk
        a_spec = pl.BlockSpec((None, tm, tk), lambda i, j, k: (k // ka, i, k % ka))
    else:
        a_spec = pl.BlockSpec((tk, tm), lambda i, j, k: (k, i)) if ta else pl.BlockSpec((tm, tk), lambda i, j, k: (i, k))
    if b_halves:
        nb_ = (N // 2) // tn
        b_spec = pl.BlockSpec((None, tk, tn), lambda i, j, k: (j // nb_, k, j % nb_))
    elif b_blocks and tb:
        per = shard // tk
        b_spec = pl.BlockSpec((1, tn, tk), lambda i, j, k: (k // per, j, k % per))
    elif b_blocks:
        per = shard // tn
        b_spec = pl.BlockSpec((1, tk, tn), lambda i, j, k: (j // per, k, j % per))
    else:
        b_spec = pl.BlockSpec((tn, tk), lambda i, j, k: (j, k)) if tb else pl.BlockSpec((tk, tn), lambda i, j, k: (k, j))
    if out_blocks:
        oper = out_blocks // tn
        o_spec = pl.BlockSpec((1, tm, tn), lambda i, j, k: (j // oper, i, j % oper))
        out_shape = jax.ShapeDtypeStruct((N // out_blocks, M, out_blocks), out_dtype)
    else:
        o_spec = pl.BlockSpec((tm, tn), lambda i, j, k: (i, j))
        out_shape = jax.ShapeDtypeStruct((M, N), out_dtype)
    in_specs = [a_spec, b_spec] + ([o_spec] if has_add else [])
    args = (a, b) + ((add,) if has_add else ())
    out_specs, semantics = o_spec, ("parallel", "parallel", "arbitrary")
    if has_norm:
        vec = pl.BlockSpec((1, N), lambda i, j, k: (0, 0))
        in_specs += [o_spec, vec, o_spec]
        args += tuple(rms_bwd)
        out_specs, out_shape = [o_spec, vec], [out_shape, jax.ShapeDtypeStruct((1, N), F32)]
        semantics = ("arbitrary", "arbitrary", "arbitrary")
    return _pcall(
        body, after, name=name, grid=(M // tm, N // tn, nk), in_specs=in_specs, out_specs=out_specs,
        out_shape=out_shape,
        scratch_shapes=[pltpu.VMEM((tm, tn), F32)] if nk > 1 else [],
        compiler_params=_params(*semantics),
    )(*args)


def _rms_fwd(x, g, *, name, tm=512, after=None):
    S, D = x.shape

    def body(x_ref, g_ref, h_ref):
        xv = x_ref[...]
        r = lax.rsqrt(jnp.mean(xv * xv, axis=-1, keepdims=True) + RMS_EPS)
        h_ref[...] = ((xv * r) * g_ref[...]).astype(h_ref.dtype)

    row = pl.BlockSpec((tm, D), lambda i: (i, 0))
    return _pcall(body, after, name=name, grid=(S // tm,), in_specs=[row, pl.BlockSpec((1, D), lambda i: (0, 0))],
                  out_specs=row, out_shape=jax.ShapeDtypeStruct((S, D), _CD), compiler_params=_params("parallel"))(x, g)


def _ffn_down_loss(act, w_down, x1, g, tgt, *, name, tm=512):
    S, D = x1.shape
    F = act.shape[1]

    def body(a_ref, b_ref, x_ref, g_ref, t_ref, loss_ref, dx_ref, dg_ref):
        xv = x_ref[...] + jnp.dot(a_ref[...].astype(_CD), b_ref[...].astype(_CD), preferred_element_type=F32)
        gv = g_ref[...]
        r = lax.rsqrt(jnp.mean(xv * xv, axis=-1, keepdims=True) + RMS_EPS)
        xh = xv * r
        err = xh * gv - t_ref[...]
        lpart = 0.5 * jnp.sum(jnp.mean(err * err, axis=-1, keepdims=True), axis=0, keepdims=True)
        dy = err * (1.0 / D)
        dxh = dy * gv
        dx_ref[...] = r * (dxh - xh * jnp.mean(dxh * xh, axis=-1, keepdims=True))
        gpart = jnp.sum(dy * xh, axis=0, keepdims=True)

        @pl.when(pl.program_id(0) == 0)
        def _():
            loss_ref[...] = lpart
            dg_ref[...] = gpart

        @pl.when(pl.program_id(0) > 0)
        def _():
            loss_ref[...] += lpart
            dg_ref[...] += gpart

    row = pl.BlockSpec((tm, D), lambda i: (i, 0))
    vec = pl.BlockSpec((1, D), lambda i: (0, 0))
    one = pl.BlockSpec((1, 1), lambda i: (0, 0))
    return _pcall(body, name=name, grid=(S // tm,),
                  in_specs=[pl.BlockSpec((tm, F), lambda i: (i, 0)), pl.BlockSpec((F, D), lambda i: (0, 0)), row, vec, row],
                  out_specs=[one, row, vec],
                  out_shape=[jax.ShapeDtypeStruct((1, 1), F32), jax.ShapeDtypeStruct((S, D), F32),
                             jax.ShapeDtypeStruct((1, D), F32)],
                  compiler_params=_params("arbitrary"))(act, w_down, x1, g, tgt)


def _sigmoid(z):
    return 1.0 / (1.0 + jnp.exp(-z))


def _gated_mix_out(gl, bg, ya, yb, w_out, x, g, *, name, tm=512):
    S, D = ya.shape

    def body(za_ref, zb_ref, ba_ref, bb_ref, ya_ref, yb_ref, w_ref, x_ref, g_ref, m_ref, x1_ref, h_ref):
        ga = _sigmoid(za_ref[...].astype(F32) + ba_ref[...])
        gb = _sigmoid(zb_ref[...].astype(F32) + bb_ref[...])
        merged = (ga * ya_ref[...].astype(F32) + gb * yb_ref[...].astype(F32)).astype(m_ref.dtype)
        m_ref[...] = merged
        x1 = x_ref[...] + jnp.dot(merged, w_ref[...].astype(_CD), preferred_element_type=F32)
        x1_ref[...] = x1
        rs = lax.rsqrt(jnp.mean(x1 * x1, axis=-1, keepdims=True) + RMS_EPS)
        h_ref[...] = ((x1 * rs) * g_ref[...]).astype(h_ref.dtype)

    lo = pl.BlockSpec((tm, D), lambda i: (i, 0))
    hi = pl.BlockSpec((tm, D), lambda i: (i, 1))
    vlo = pl.BlockSpec((1, D), lambda i: (0, 0))
    vhi = pl.BlockSpec((1, D), lambda i: (0, 1))
    whole = pl.BlockSpec((D, D), lambda i: (0, 0))
    return _pcall(body, name=name, grid=(S // tm,), in_specs=[lo, hi, vlo, vhi, lo, lo, whole, lo, vlo],
                  out_specs=[lo, lo, lo],
                  out_shape=[jax.ShapeDtypeStruct((S, D), _CD), jax.ShapeDtypeStruct((S, D), F32),
                             jax.ShapeDtypeStruct((S, D), _CD)],
                  compiler_params=_params("parallel"))(gl, gl, bg, bg, ya, yb, w_out, x, g)


def _gate_bwd(dx1, w_out, gl, bg, ya, yb, *, name, tm=512):
    S, D = ya.shape
    nt = (((1,), (1,)), ((), ()))

    def body(dx_ref, w_ref, za_ref, zb_ref, ba_ref, bb_ref, ya_ref, yb_ref, dya_ref, dyb_ref, dgl_ref, dbg_ref):
        dmv = lax.dot_general(dx_ref[...].astype(_CD), w_ref[...].astype(_CD), nt, preferred_element_type=F32)
        ga = _sigmoid(za_ref[...].astype(F32) + ba_ref[...])
        gb = _sigmoid(zb_ref[...].astype(F32) + bb_ref[...])
        dya_ref[...] = (dmv * ga).astype(dya_ref.dtype)
        dyb_ref[...] = (dmv * gb).astype(dyb_ref.dtype)
        dza = dmv * ya_ref[...].astype(F32) * ga * (1.0 - ga)
        dzb = dmv * yb_ref[...].astype(F32) * gb * (1.0 - gb)
        dgl_ref[:, :D] = dza.astype(dgl_ref.dtype)
        dgl_ref[:, D:] = dzb.astype(dgl_ref.dtype)
        pa = jnp.sum(dza, axis=0, keepdims=True)
        pb = jnp.sum(dzb, axis=0, keepdims=True)

        @pl.when(pl.program_id(0) == 0)
        def _():
            dbg_ref[:, :D] = pa
            dbg_ref[:, D:] = pb

        @pl.when(pl.program_id(0) > 0)
        def _():
            dbg_ref[:, :D] += pa
            dbg_ref[:, D:] += pb

    lo = pl.BlockSpec((tm, D), lambda i: (i, 0))
    hi = pl.BlockSpec((tm, D), lambda i: (i, 1))
    vlo = pl.BlockSpec((1, D), lambda i: (0, 0))
    vhi = pl.BlockSpec((1, D), lambda i: (0, 1))
    wide = pl.BlockSpec((tm, 2 * D), lambda i: (i, 0))
    vwide = pl.BlockSpec((1, 2 * D), lambda i: (0, 0))
    whole = pl.BlockSpec((D, D), lambda i: (0, 0))
    return _pcall(body, name=name, grid=(S // tm,), in_specs=[lo, whole, lo, hi, vlo, vhi, lo, lo],
                  out_specs=[lo, lo, wide, vwide],
                  out_shape=[jax.ShapeDtypeStruct((S, D), _CD), jax.ShapeDtypeStruct((S, D), _CD),
                             jax.ShapeDtypeStruct((S, 2 * D), _CD), jax.ShapeDtypeStruct((1, 2 * D), F32)],
                  compiler_params=_params("arbitrary"))(dx1, w_out, gl, gl, bg, bg, ya, yb)


def _ffn_in_act(h2, w_blocks, *, name, tm=512):
    S, D = h2.shape
    _, _, C = w_blocks.shape

    def body(a_ref, bg_ref, bu_ref, g_ref, u_ref, o_ref):
        av = a_ref[...].astype(_CD)
        gv = jnp.dot(av, bg_ref[0].astype(_CD), preferred_element_type=F32)
        uv = jnp.dot(av, bu_ref[0].astype(_CD), preferred_element_type=F32)
        g_ref[...] = gv.astype(g_ref.dtype)
        u_ref[...] = uv.astype(u_ref.dtype)
        o_ref[...] = (gv * _sigmoid(gv) * uv).astype(o_ref.dtype)

    out = pl.BlockSpec((tm, C), lambda i, j: (i, j))
    shp = jax.ShapeDtypeStruct((S, 2 * C), _CD)
    return _pcall(body, name=name, grid=(S // tm, 2),
                  in_specs=[pl.BlockSpec((tm, D), lambda i, j: (i, 0)), pl.BlockSpec((1, D, C), lambda i, j: (j, 0, 0)),
                            pl.BlockSpec((1, D, C), lambda i, j: (2 + j, 0, 0))],
                  out_specs=[out, out, out], out_shape=[shp, shp, shp],
                  compiler_params=_params("parallel", "arbitrary"))(h2, w_blocks, w_blocks)


def _d_swiglu(dx, w_down, gate, up, *, name, tm=512, tn=1408):
    S, D = dx.shape
    F = w_down.shape[0]
    nt = (((1,), (1,)), ((), ()))

    def body(a_ref, b_ref, g_ref, u_ref, o_ref):
        dv = lax.dot_general(a_ref[...].astype(_CD), b_ref[...].astype(_CD), nt, preferred_element_type=F32)
        gv = g_ref[...].astype(F32)
        sg = _sigmoid(gv)
        o_ref[0] = (dv * u_ref[...].astype(F32) * (sg * (1.0 + gv * (1.0 - sg)))).astype(o_ref.dtype)
        o_ref[1] = (dv * (gv * sg)).astype(o_ref.dtype)

    tile = pl.BlockSpec((tm, tn), lambda i, j: (i, j))
    return _pcall(body, name=name, grid=(S // tm, F // tn),
                  in_specs=[pl.BlockSpec((tm, D), lambda i, j: (i, 0)), pl.BlockSpec((tn, D), lambda i, j: (j, 0)),
                            tile, tile],
                  out_specs=pl.BlockSpec((2, tm, tn), lambda i, j: (0, i, j)),
                  out_shape=jax.ShapeDtypeStruct((2, S, F), _CD),
                  compiler_params=_params("parallel", "arbitrary"))(dx, w_down, gate, up)


def _split3(x):
    hi = x.astype(jnp.bfloat16)
    r1 = x - hi.astype(F32)
    mid = r1.astype(jnp.bfloat16)
    lo = (r1 - mid.astype(F32)).astype(jnp.bfloat16)
    return hi, mid, lo


def _ones_dot_left(ones, x):
    return sum(jnp.dot(ones, p, preferred_element_type=F32) for p in _split3(x))


def _ones_dot_right(x, ones):
    return sum(jnp.dot(p, ones, preferred_element_type=F32) for p in _split3(x))


def _head_sum(x):
    n = x.shape[1]
    r = lax.broadcasted_iota(jnp.int32, (n, n), 0) // HEAD_DIM
    c = lax.broadcasted_iota(jnp.int32, (n, n), 1) // HEAD_DIM
    return _ones_dot_right(x, (r == c).astype(jnp.bfloat16))


def _log_sigmoid(z):
    e = jnp.exp(-jnp.abs(z))
    t = 1.0 + e
    log1p_e = jnp.where(t == 1.0, e, jnp.log(t) * (e / jnp.where(t == 1.0, 1.0, t - 1.0)))
    return jnp.minimum(z, 0.0) - log1p_e


def _fox_cumsum(zf, bf, *, name):
    S, W = zf.shape
    nb = S // 128

    def body(z_ref, b_ref, c_ref):
        tri = (lax.broadcasted_iota(jnp.int32, (128, 128), 0) >= lax.broadcasted_iota(jnp.int32, (128, 128), 1))
        tri = tri.astype(jnp.bfloat16)

        def step(i, carry):
            rows = pl.ds(pl.multiple_of(i * 128, 128), 128)
            lf = _log_sigmoid(z_ref[rows, :] + b_ref[...])
            cb = _ones_dot_left(tri, lf) + carry
            c_ref[rows, :] = cb
            return cb[127:128, :]

        lax.fori_loop(0, nb, step, jnp.zeros((1, W), F32))

    return _pcall(body, name=name, out_shape=jax.ShapeDtypeStruct((S, W), F32),
                  compiler_params=pltpu.CompilerParams(vmem_limit_bytes=VMEM_LIMIT))(zf, bf)


def _fox_cumsum_bwd(dc, zf, bf, *, name):
    S, W = zf.shape
    nb = S // 128

    def body(dc_ref, z_ref, b_ref, dz_ref, db_ref):
        tri = (lax.broadcasted_iota(jnp.int32, (128, 128), 0) <= lax.broadcasted_iota(jnp.int32, (128, 128), 1))
        tri = tri.astype(jnp.bfloat16)

        def step(k, carry):
            tail, acc = carry
            i = nb - 1 - k
            rows = pl.ds(pl.multiple_of(i * 128, 128), 128)
            dlf = _ones_dot_left(tri, dc_ref[rows, :]) + tail
            dz = dlf * _sigmoid(-(z_ref[rows, :] + b_ref[...]))
            dz_ref[rows, :] = dz
            return dlf[0:1, :], acc + jnp.sum(dz, axis=0, keepdims=True)

        _, acc = lax.fori_loop(0, nb, step, (jnp.zeros((1, W), F32), jnp.zeros((1, W), F32)))
        db_ref[...] = acc

    return _pcall(body, name=name,
                  out_shape=[jax.ShapeDtypeStruct((S, W), F32), jax.ShapeDtypeStruct((1, W), F32)],
                  compiler_params=pltpu.CompilerParams(vmem_limit_bytes=VMEM_LIMIT))(dc, zf, bf)


def _proj_dil(h, w_qkv, *, name, tm=1024):
    S, D = h.shape
    tn = DIL_WIDTH

    def body(a_ref, b_ref, *rest):
        outs, acc = rest[:N_DIL_GROUPS], rest[N_DIL_GROUPS]
        prod = jnp.dot(a_ref[...].astype(_CD), b_ref[...].astype(_CD), preferred_element_type=F32)
        for k in range(tn // LANES):
            acc[k] = prod[:, k * LANES:(k + 1) * LANES]
        for g, (_, d) in enumerate(DIL_PAIRS):
            for half in range(DIL_OUT // LANES):
                k = g * (DIL_OUT // LANES) + half
                cols = slice(half * LANES, (half + 1) * LANES)
                for r in range(d):
                    rows = pl.ds(r, tm // d, stride=d) if d > 1 else slice(None)
                    outs[g][0, r, :, cols] = acc[k, rows, :].astype(outs[g].dtype)

    out_specs = [pl.BlockSpec((1, d, tm // d, DIL_OUT), lambda i, j: (j, 0, i, 0)) for _, d in DIL_PAIRS]
    out_shape = [jax.ShapeDtypeStruct((3, d, S // d, DIL_OUT), _CD) for _, d in DIL_PAIRS]
    outs = _pcall(body, name=name, grid=(S // tm, 3),
                  in_specs=[pl.BlockSpec((tm, D), lambda i, j: (i, 0)), pl.BlockSpec((D, tn), lambda i, j: (0, j))],
                  out_specs=out_specs, out_shape=out_shape, scratch_shapes=[pltpu.VMEM((tn // LANES, tm, LANES), F32)],
                  compiler_params=_params("parallel", "arbitrary"))(h, w_qkv)
    return [o.reshape(3, S, DIL_OUT) for o in outs]


def _dil_start(block, S, dilation):
    sub = S // dilation
    u0 = block * DIL_W
    return (u0 % sub) * dilation + u0 // sub


def _dil_slopes(group):
    h = np.arange(1, N_DIL_GROUPS * DIL_HEADS + 1, dtype=np.float32)
    s = (np.float32(2.0) ** (np.float32(-8.0) * h / np.float32(N_DIL_GROUPS * DIL_HEADS))).astype(np.float32)
    return [float(v) for v in s.reshape(N_DIL_GROUPS, DIL_HEADS)[group]]


def _dil_tiles(i, n, blocks_per_seq):
    qi = lax.broadcasted_iota(jnp.int32, (DIL_W, 2 * DIL_W), 0)
    kj = lax.broadcasted_iota(jnp.int32, (DIL_W, 2 * DIL_W), 1)
    rel = qi + DIL_W - kj
    first = ((4 * n + i) % blocks_per_seq) == 0
    valid = jnp.logical_and(jnp.logical_and(rel >= 0, rel <= DIL_W), jnp.logical_or(kj >= DIL_W, jnp.logical_not(first)))
    return valid, rel.astype(F32)


def _dil_window(cur_ref, prev_ref, i, cols):
    if i > 0:
        return cur_ref[(i - 1) * DIL_W:(i + 1) * DIL_W, cols]
    return jnp.concatenate([prev_ref[:, cols], cur_ref[:DIL_W, cols]], axis=0)


CHUNK = 4 * DIL_W


def _dil_rows(block, S, dilation):
    start = _dil_start(block, S, dilation)
    return pl.ds(start, DIL_W, stride=dilation) if dilation > 1 else pl.ds(start, DIL_W)


def SPLIT(S):
    return (DIL_OUT // LANES, S, LANES)


def _dil_fwd(qkv, group, *, name):
    S = qkv.shape[1]
    dilation = DIL_PAIRS[group][1]
    bps = (S // dilation) // DIL_W
    slopes = _dil_slopes(group)
    nt = (((1,), (1,)), ((), ()))

    def body(q_ref, k_ref, v_ref, kp_ref, vp_ref, on_ref, ln_ref, o_ref, l_ref):
        n = pl.program_id(0)
        for i in range(4):
            valid, rel = _dil_tiles(i, n, bps)
            rows = slice(i * DIL_W, (i + 1) * DIL_W)
            for h in range(DIL_HEADS):
                cols = slice(h * HEAD_DIM, (h + 1) * HEAD_DIM)
                qh = q_ref[rows, cols]
                k2, v2 = _dil_window(k_ref, kp_ref, i, cols), _dil_window(v_ref, vp_ref, i, cols)
                s = lax.dot_general(qh, k2, nt, preferred_element_type=F32) * ATTN_SCALE - (slopes[h] * dilation) * rel
                s = jnp.where(valid, s, NEG_INF)
                m = jnp.max(s, axis=-1, keepdims=True)
                p = jnp.exp(s - m)
                den = jnp.sum(p, axis=-1, keepdims=True)
                acc = jnp.dot(p.astype(_CD), v2, preferred_element_type=F32)
                o_ref[rows, cols] = acc / den
                l_ref[rows, cols] = jnp.broadcast_to(m + jnp.log(den), (DIL_W, HEAD_DIM))
        for i in range(4):
            rows = slice(i * DIL_W, (i + 1) * DIL_W)
            nat = _dil_rows(4 * n + i, S, dilation)
            for half in range(DIL_OUT // LANES):
                cols = slice(half * LANES, (half + 1) * LANES)
                on_ref[half, nat, :] = o_ref[rows, cols]
                ln_ref[half, nat, :] = l_ref[rows, cols]

    def cur(which):
        return pl.BlockSpec((None, CHUNK, DIL_OUT), lambda n: (which, n, 0))

    def prev(which):
        return pl.BlockSpec((None, DIL_W, DIL_OUT), lambda n: (which, jnp.maximum(4 * n - 1, 0), 0))

    whole = pl.BlockSpec(SPLIT(S), lambda n: (0, 0, 0))
    return _pcall(body, name=name, grid=(S // CHUNK,), in_specs=[cur(0), cur(1), cur(2), prev(1), prev(2)],
                  out_specs=[whole, whole],
                  out_shape=[jax.ShapeDtypeStruct(SPLIT(S), F32), jax.ShapeDtypeStruct(SPLIT(S), F32)],
                  scratch_shapes=[pltpu.VMEM((CHUNK, DIL_OUT), F32), pltpu.VMEM((CHUNK, DIL_OUT), F32)],
                  compiler_params=_params("arbitrary"))(qkv, qkv, qkv, qkv, qkv)


STAT_OFFSET = HEAD_DIM // 2


def _dil_bwd(qkv, stats, do, group, *, name):
    S = qkv.shape[1]
    dilation = DIL_PAIRS[group][1]
    bps = (S // dilation) // DIL_W
    slopes = _dil_slopes(group)
    nchunk = S // CHUNK
    nt = (((1,), (1,)), ((), ()))
    tn = (((0,), (0,)), ((), ()))

    def body(q_ref, k_ref, v_ref, kp_ref, vp_ref, ln_ref, don_ref, dqn_ref, dkn_ref, dvn_ref,
             dk_s, dv_s, l_ref, do_ref, dq_ref):
        step = pl.program_id(0)
        n = nchunk - 1 - step
        for i in range(4):
            rows = slice(i * DIL_W, (i + 1) * DIL_W)
            nat = _dil_rows(4 * n + i, S, dilation)
            for half in range(DIL_OUT // LANES):
                cols = slice(half * LANES, (half + 1) * LANES)
                l_ref[rows, cols] = ln_ref[half, nat, :]
                do_ref[rows, cols] = don_ref[half, nat, :]

        @pl.when(step == 0)
        def _():
            dk_s[:, CHUNK:] = jnp.zeros((DIL_OUT, DIL_W), F32)
            dv_s[:, CHUNK:] = jnp.zeros((DIL_OUT, DIL_W), F32)

        dk_s[:, :CHUNK] = jnp.zeros((DIL_OUT, CHUNK), F32)
        dv_s[:, :CHUNK] = jnp.zeros((DIL_OUT, CHUNK), F32)
        for i in range(4):
            valid, rel = _dil_tiles(i, n, bps)
            rows = slice(i * DIL_W, (i + 1) * DIL_W)
            window = slice(i * DIL_W, (i + 2) * DIL_W)
            for h in range(DIL_HEADS):
                cols = slice(h * HEAD_DIM, (h + 1) * HEAD_DIM)
                qh = q_ref[rows, cols]
                k2, v2 = _dil_window(k_ref, kp_ref, i, cols), _dil_window(v_ref, vp_ref, i, cols)
                lh = l_ref[rows, h * HEAD_DIM:h * HEAD_DIM + 1]
                shift = l_ref[rows, h * HEAD_DIM + STAT_OFFSET:h * HEAD_DIM + STAT_OFFSET + 1]
                s = lax.dot_general(qh, k2, nt, preferred_element_type=F32) * ATTN_SCALE - (slopes[h] * dilation) * rel
                p = jnp.exp(jnp.where(valid, s, NEG_INF) - lh)
                dob = do_ref[rows, cols].astype(_CD)
                ds = p * (lax.dot_general(dob, v2, nt, preferred_element_type=F32) + shift)
                dsb = (ds * ATTN_SCALE).astype(_CD)
                dq_ref[rows, cols] = jnp.dot(dsb, k2, preferred_element_type=F32)
                dk_s[cols, window] += lax.dot_general(qh, dsb, tn, preferred_element_type=F32)
                dv_s[cols, window] += lax.dot_general(dob, p.astype(_CD), tn, preferred_element_type=F32)
        for i in range(4):
            rows = slice(i * DIL_W, (i + 1) * DIL_W)
            done = slice((i + 1) * DIL_W, (i + 2) * DIL_W)
            nat = _dil_rows(4 * n + i, S, dilation)
            dkb, dvb = dk_s[:, done].T, dv_s[:, done].T
            for half in range(DIL_OUT // LANES):
                cols = slice(half * LANES, (half + 1) * LANES)
                dqn_ref[half, nat, :] = dq_ref[rows, cols]
                dkn_ref[half, nat, :] = dkb[:, cols]
                dvn_ref[half, nat, :] = dvb[:, cols]
        dk_s[:, CHUNK:] = dk_s[:, :DIL_W]
        dv_s[:, CHUNK:] = dv_s[:, :DIL_W]

    def cur(which):
        return pl.BlockSpec((None, CHUNK, DIL_OUT), lambda s: (which, nchunk - 1 - s, 0))

    def prev(which):
        return pl.BlockSpec((None, DIL_W, DIL_OUT), lambda s: (which, jnp.maximum(4 * (nchunk - 1 - s) - 1, 0), 0))

    whole = pl.BlockSpec(SPLIT(S), lambda s: (0, 0, 0))
    shp = jax.ShapeDtypeStruct(SPLIT(S), F32)
    tile = pltpu.VMEM((CHUNK, DIL_OUT), F32)
    return _pcall(body, name=name, grid=(nchunk,),
                  in_specs=[cur(0), cur(1), cur(2), prev(1), prev(2), whole, whole],
                  out_specs=[whole, whole, whole], out_shape=[shp, shp, shp],
                  scratch_shapes=[pltpu.VMEM((DIL_OUT, CHUNK + DIL_W), F32), pltpu.VMEM((DIL_OUT, CHUNK + DIL_W), F32),
                                  tile, tile, tile],
                  compiler_params=pltpu.CompilerParams(dimension_semantics=("arbitrary",),
                                                       vmem_limit_bytes=VMEM_LIMIT_RESIDENT))(
        qkv, qkv, qkv, qkv, qkv, stats, do)


def _dil_mix_fwd(os_, ls_, *, name, tm=512):
    nh, S, _ = os_[0].shape

    def body(o0, o1, o2, l0, l1, l2, out_ref):
        for half in range(nh):
            ls = [l0[half], l1[half], l2[half]]
            m = jnp.maximum(jnp.maximum(ls[0], ls[1]), ls[2])
            es = [jnp.exp(l - m) for l in ls]
            den = es[0] + es[1] + es[2]
            mixed = (es[0] * o0[half] + es[1] * o1[half] + es[2] * o2[half]) / den
            out_ref[:, half * LANES:(half + 1) * LANES] = mixed.astype(out_ref.dtype)

    halves = pl.BlockSpec((nh, tm, LANES), lambda i: (0, i, 0))
    row = pl.BlockSpec((tm, nh * LANES), lambda i: (i, 0))
    return _pcall(body, name=name, grid=(S // tm,), in_specs=[halves] * 6, out_specs=row,
                  out_shape=jax.ShapeDtypeStruct((S, nh * LANES), _CD), compiler_params=_params("parallel"))(*os_, *ls_)


def _dil_mix_bwd(doa, os_, ls_, *, name, tm=512, after=None):
    nh, S, _ = os_[0].shape

    def body(d_ref, o0, o1, o2, l0, l1, l2, do0, do1, do2, st0, st1, st2):
        first = lax.broadcasted_iota(jnp.int32, (tm, LANES), 1) % HEAD_DIM < STAT_OFFSET
        for half in range(nh):
            dv = d_ref[:, half * LANES:(half + 1) * LANES]
            ls = [l0[half], l1[half], l2[half]]
            m = jnp.maximum(jnp.maximum(ls[0], ls[1]), ls[2])
            es = [jnp.exp(l - m) for l in ls]
            den = es[0] + es[1] + es[2]
            al = [e / den for e in es]
            da = [_head_sum(dv * o[half]) for o in (o0, o1, o2)]
            mean = al[0] * da[0] + al[1] * da[1] + al[2] * da[2]
            for a, l, do_ref, st_ref in zip(al, ls, (do0, do1, do2), (st0, st1, st2)):
                do_ref[half] = a * dv
                st_ref[half] = jnp.where(first, l, -a * mean)

    halves = pl.BlockSpec((nh, tm, LANES), lambda i: (0, i, 0))
    row = pl.BlockSpec((tm, nh * LANES), lambda i: (i, 0))
    shp = jax.ShapeDtypeStruct((nh, S, LANES), F32)
    return _pcall(body, after, name=name, grid=(S // tm,), in_specs=[row] + [halves] * 6, out_specs=[halves] * 6,
                  out_shape=[shp] * 6, compiler_params=_params("parallel"))(doa, *os_, *ls_)


FOX_T = 512


PACK = 2 * HEAD_DIM
HEAD_PAIRS = N_FOX_HEADS // 2
FOX_HPS = 8
Q_BLOCK0 = 0
K_BLOCK0 = FOX_WIDTH // PACK
V_BLOCK0 = 2 * FOX_WIDTH // PACK


def _pieces(x):
    hi = x.astype(jnp.bfloat16).astype(F32)
    r = x - hi
    mid = r.astype(jnp.bfloat16).astype(F32)
    lo = (r - mid).astype(jnp.bfloat16).astype(F32)
    return [hi, mid, lo]


def _extras(first, second, rows):
    lane = lax.broadcasted_iota(jnp.int32, (rows, HEAD_DIM), 1)
    out = jnp.zeros((rows, HEAD_DIM), F32)
    for base, triple in ((0, first), (3, second)):
        if all(isinstance(v, float) for v in triple) and len(set(triple)) == 1:
            if triple[0] != 0.0:
                out = jnp.where(jnp.logical_and(lane >= base, lane < base + 3), triple[0], out)
        else:
            for idx, val in enumerate(triple):
                out = jnp.where(lane == base + idx, val, out)
    return out


def _head_column(c, h):
    lane = lax.broadcasted_iota(jnp.int32, c.shape, 1)
    return jnp.sum(jnp.where(lane == h, c, 0.0), axis=1, keepdims=True)


ONES3 = [1.0, 1.0, 1.0]
ZEROS3 = [0.0, 0.0, 0.0]


def _fox_pack_fwd(qkv, c, *, name, tm=1024):
    S = qkv.shape[0]

    def body(q_ref, k_ref, v_ref, c_ref, qo_ref, ko_ref, vo_ref):
        hp = pl.program_id(1)
        cv = c_ref[...]
        v_extras = jnp.where(lax.broadcasted_iota(jnp.int32, (tm, HEAD_DIM), 1) < 3, 1.0, 0.0).astype(vo_ref.dtype)
        for hh in range(2):
            ch = _pieces(_head_column(cv, 2 * hp + hh))
            src = slice(hh * HEAD_DIM, (hh + 1) * HEAD_DIM)
            lo = slice(hh * PACK, hh * PACK + HEAD_DIM)
            hi = slice(hh * PACK + HEAD_DIM, (hh + 1) * PACK)
            qo_ref[:, lo] = (q_ref[:, src].astype(F32) * ATTN_SCALE).astype(qo_ref.dtype)
            qo_ref[:, hi] = _extras(ch, ONES3, tm).astype(qo_ref.dtype)
            ko_ref[:, lo] = k_ref[:, src]
            ko_ref[:, hi] = _extras(ONES3, [-p for p in ch], tm).astype(ko_ref.dtype)
            vo_ref[:, lo] = v_ref[:, src]
            vo_ref[:, hi] = v_extras

    def src(block0):
        return pl.BlockSpec((tm, PACK), lambda i, hp: (i, block0 + hp))

    out = pl.BlockSpec((tm, 2 * PACK), lambda i, hp: (i, hp))
    shp = jax.ShapeDtypeStruct((S, N_FOX_HEADS * PACK), _CD)
    return _pcall(body, name=name, grid=(S // tm, HEAD_PAIRS),
                  in_specs=[src(Q_BLOCK0), src(K_BLOCK0), src(V_BLOCK0), pl.BlockSpec((tm, PACK), lambda i, hp: (i, 0))],
                  out_specs=[out, out, out], out_shape=[shp, shp, shp],
                  compiler_params=_params("parallel", "parallel"))(qkv, qkv, qkv, c)


def _fox_fwd(qp, kp, vp, *, name):
    S = qp.shape[0]
    nt = S // FOX_T
    nt_dims = (((1,), (1,)), ((), ()))
    tn_dims = (((0,), (0,)), ((), ()))

    def body(i_tab, j_tab, q_ref, k_ref, v_ref, o_ref, l_ref, m_s, acc_s):
        t = pl.program_id(1)
        i, j = i_tab[t], j_tab[t]

        @pl.when(j == 0)
        def _():
            m_s[...] = jnp.full((FOX_HPS, 1, FOX_T), NEG_INF, F32)
            acc_s[...] = jnp.zeros((FOX_HPS, PACK, FOX_T), F32)

        def tile(diagonal):
            for hh in range(FOX_HPS):
                cols = slice(hh * PACK, (hh + 1) * PACK)
                st = lax.dot_general(k_ref[:, cols], q_ref[:, cols], nt_dims, preferred_element_type=F32)
                if diagonal:
                    key = lax.broadcasted_iota(jnp.int32, (FOX_T, FOX_T), 0)
                    qry = lax.broadcasted_iota(jnp.int32, (FOX_T, FOX_T), 1)
                    st = jnp.where(key <= qry, st, NEG_INF)
                m_old = m_s[hh]
                m_new = jnp.maximum(m_old, jnp.max(st, axis=0, keepdims=True))
                pt = jnp.exp(st - m_new)
                acc_s[hh] = jnp.exp(m_old - m_new) * acc_s[hh] + lax.dot_general(
                    v_ref[:, cols], pt.astype(_CD), tn_dims, preferred_element_type=F32)
                m_s[hh] = m_new

        @pl.when(j < i)
        def _():
            tile(False)

        @pl.when(j == i)
        def _():
            tile(True)
            for hh in range(FOX_HPS):
                acc = acc_s[hh]
                den = acc[HEAD_DIM:HEAD_DIM + 1, :]
                cols = slice(hh * HEAD_DIM, (hh + 1) * HEAD_DIM)
                o_ref[:, cols] = (acc[:HEAD_DIM, :] / den).T
                l_ref[:, cols] = jnp.broadcast_to(m_s[hh] + jnp.log(den), (HEAD_DIM, FOX_T)).T

    pairs = [(i, j) for i in range(nt) for j in range(i + 1)]
    i_tab = jnp.asarray([p[0] for p in pairs], jnp.int32)
    j_tab = jnp.asarray([p[1] for p in pairs], jnp.int32)
    qs = pl.BlockSpec((FOX_T, FOX_HPS * PACK), lambda hp, t, it, jt: (it[t], hp))
    ks = pl.BlockSpec((FOX_T, FOX_HPS * PACK), lambda hp, t, it, jt: (jt[t], hp))
    os_ = pl.BlockSpec((FOX_T, FOX_HPS * HEAD_DIM), lambda hp, t, it, jt: (it[t], hp))
    shp = jax.ShapeDtypeStruct((S, FOX_WIDTH), F32)
    grid_spec = pltpu.PrefetchScalarGridSpec(
        num_scalar_prefetch=2, grid=(N_FOX_HEADS // FOX_HPS, len(pairs)), in_specs=[qs, ks, ks], out_specs=[os_, os_],
        scratch_shapes=[pltpu.VMEM((FOX_HPS, 1, FOX_T), F32), pltpu.VMEM((FOX_HPS, PACK, FOX_T), F32)])
    return _pcall(body, name=name, grid_spec=grid_spec, out_shape=[shp, shp],
                  compiler_params=_params("parallel", "arbitrary"))(i_tab, j_tab, qp, kp, vp)


def _fox_pack_bwd(qkv, c, o, lse, do, *, name, tm=1024, after=None):
    S = qkv.shape[0]

    def body(q_ref, c_ref, o_ref, l_ref, do_ref, qo_ref, do_out_ref):
        hp = pl.program_id(1)
        cv = c_ref[...]
        for hh in range(2):
            src = slice(hh * HEAD_DIM, (hh + 1) * HEAD_DIM)
            lo = slice(hh * PACK, hh * PACK + HEAD_DIM)
            hi = slice(hh * PACK + HEAD_DIM, (hh + 1) * PACK)
            shift = _head_column(cv, 2 * hp + hh) - l_ref[:, hh * HEAD_DIM:hh * HEAD_DIM + 1]
            dov = do_ref[:, src]
            dsum = jnp.sum(dov * o_ref[:, src], axis=-1, keepdims=True)
            qo_ref[:, lo] = (q_ref[:, src].astype(F32) * ATTN_SCALE).astype(qo_ref.dtype)
            qo_ref[:, hi] = _extras(_pieces(shift), ONES3, tm).astype(qo_ref.dtype)
            do_out_ref[:, lo] = dov.astype(do_out_ref.dtype)
            do_out_ref[:, hi] = _extras(_pieces(-dsum), ZEROS3, tm).astype(do_out_ref.dtype)

    pair = pl.BlockSpec((tm, PACK), lambda i, hp: (i, hp))
    out = pl.BlockSpec((tm, 2 * PACK), lambda i, hp: (i, hp))
    shp = jax.ShapeDtypeStruct((S, N_FOX_HEADS * PACK), _CD)
    return _pcall(body, after, name=name, grid=(S // tm, HEAD_PAIRS),
                  in_specs=[pl.BlockSpec((tm, PACK), lambda i, hp: (i, Q_BLOCK0 + hp)),
                            pl.BlockSpec((tm, PACK), lambda i, hp: (i, 0)), pair, pair, pair],
                  out_specs=[out, out], out_shape=[shp, shp],
                  compiler_params=_params("parallel", "parallel"))(qkv, c, o, lse, do)


def _fox_bwd(qp, kp, vp, dop, *, name):
    S = qp.shape[0]
    nt = S // FOX_T
    nt_dims = (((1,), (1,)), ((), ()))
    tn_dims = (((0,), (0,)), ((), ()))

    def body(i_tab, j_tab, q_ref, k_ref, v_ref, do_ref, dq_ref, dk_ref, dv_ref, dc_ref, dr_ref,
             dq_s, dk_s, dv_s, dc_s, dr_s):
        t = pl.program_id(1)
        i, j = i_tab[t], j_tab[t]

        @pl.when(t == 0)
        def _():
            dq_s[...] = jnp.zeros((S, FOX_HPS * PACK), F32)
            dr_s[...] = jnp.zeros((FOX_HPS, 1, S), F32)

        @pl.when(i == j)
        def _():
            dk_s[...] = jnp.zeros((FOX_T, FOX_HPS * PACK), F32)
            dv_s[...] = jnp.zeros((FOX_T, FOX_HPS * PACK), F32)
            dc_s[...] = jnp.zeros((FOX_HPS, FOX_T, 1), F32)

        def tile(diagonal):
            rows = pl.ds(pl.multiple_of(i * FOX_T, FOX_T), FOX_T)
            for hh in range(FOX_HPS):
                cols = slice(hh * PACK, (hh + 1) * PACK)
                qv, kv, vv, dov = q_ref[:, cols], k_ref[:, cols], v_ref[:, cols], do_ref[:, cols]
                pt = jnp.exp(lax.dot_general(kv, qv, nt_dims, preferred_element_type=F32))
                if diagonal:
                    key = lax.broadcasted_iota(jnp.int32, (FOX_T, FOX_T), 0)
                    qry = lax.broadcasted_iota(jnp.int32, (FOX_T, FOX_T), 1)
                    pt = jnp.where(key <= qry, pt, 0.0)
                dst = pt * lax.dot_general(vv, dov, nt_dims, preferred_element_type=F32)
                dsb = dst.astype(_CD)
                dc_s[hh] += jnp.sum(dst, axis=1, keepdims=True)
                dr_s[hh, :, rows] += jnp.sum(dst, axis=0, keepdims=True)
                dv_s[:, cols] += jnp.dot(pt.astype(_CD), dov, preferred_element_type=F32)
                dk_s[:, cols] += jnp.dot(dsb, qv, preferred_element_type=F32)
                dq_s[rows, cols] += lax.dot_general(dsb, kv, tn_dims, preferred_element_type=F32)

        @pl.when(i > j)
        def _():
            tile(False)

        @pl.when(i == j)
        def _():
            tile(True)

        @pl.when(i == nt - 1)
        def _():
            for hh in range(FOX_HPS):
                src = slice(hh * PACK, hh * PACK + HEAD_DIM)
                dst_cols = slice(hh * HEAD_DIM, (hh + 1) * HEAD_DIM)
                dk_ref[:, dst_cols] = dk_s[:, src].astype(dk_ref.dtype)
                dv_ref[:, dst_cols] = dv_s[:, src].astype(dv_ref.dtype)
                dc_ref[:, dst_cols] = jnp.broadcast_to(dc_s[hh], (FOX_T, HEAD_DIM))

        @pl.when(t == len(pairs) - 1)
        def _():
            for hh in range(FOX_HPS):
                dq_ref[:, hh * HEAD_DIM:(hh + 1) * HEAD_DIM] = (
                    dq_s[:, hh * PACK:hh * PACK + HEAD_DIM] * ATTN_SCALE).astype(dq_ref.dtype)
            dr_ref[...] = dr_s[...]

    pairs = [(i, j) for j in range(nt) for i in range(j, nt)]
    i_tab = jnp.asarray([p[0] for p in pairs], jnp.int32)
    j_tab = jnp.asarray([p[1] for p in pairs], jnp.int32)
    wide, narrow = FOX_HPS * PACK, FOX_HPS * HEAD_DIM
    qs = pl.BlockSpec((FOX_T, wide), lambda hp, t, it, jt: (it[t], hp))
    ks = pl.BlockSpec((FOX_T, wide), lambda hp, t, it, jt: (jt[t], hp))
    whole = pl.BlockSpec((S, narrow), lambda hp, t, it, jt: (0, hp))
    cs = pl.BlockSpec((FOX_T, narrow), lambda hp, t, it, jt: (jt[t], hp))
    rs = pl.BlockSpec((FOX_HPS, 1, S), lambda hp, t, it, jt: (hp, 0, 0))
    shp = jax.ShapeDtypeStruct((S, FOX_WIDTH), _CD)
    grid_spec = pltpu.PrefetchScalarGridSpec(
        num_scalar_prefetch=2, grid=(N_FOX_HEADS // FOX_HPS, len(pairs)), in_specs=[qs, ks, ks, qs],
        out_specs=[whole, cs, cs, cs, rs],
        scratch_shapes=[pltpu.VMEM((S, wide), F32), pltpu.VMEM((FOX_T, wide), F32),
                        pltpu.VMEM((FOX_T, wide), F32), pltpu.VMEM((FOX_HPS, FOX_T, 1), F32),
                        pltpu.VMEM((FOX_HPS, 1, S), F32)])
    return _pcall(body, name=name, grid_spec=grid_spec,
                  out_shape=[shp, shp, shp, jax.ShapeDtypeStruct((S, FOX_WIDTH), F32),
                             jax.ShapeDtypeStruct((N_FOX_HEADS, 1, S), F32)],
                  compiler_params=_params("parallel", "arbitrary"))(i_tab, j_tab, qp, kp, vp, dop)


def _layer_step(x, tgt, w, p, late_weights=None, grad_sink=None, after=None, first_weights=None):
    S = x.shape[0]
    after_norm, after_proj = after if after is not None else (None, None)
    h = _rms_fwd(x, p["norm_mix_g"], name="rms_mix", after=after_norm)
    if first_weights is not None:
        w = {**w, **first_weights(h)}
    qkv = _mm(h, w["qkv"][:, 3 * DIL_WIDTH:], name="proj_fox", out_dtype=_CD, tn=768, tm=2048, after=after_proj)
    dil_qkv = _proj_dil(h, w["qkv"], name="proj_dil")
    zf = _mm(h, w["f"], name="proj_f")
    gl = _mm(h, w["g"], name="proj_gate", tn=1024, out_dtype=_CD)

    dil_o, dil_l = [], []
    for g in range(N_DIL_GROUPS):
        og, lg = _dil_fwd(dil_qkv[g], g, name=f"dil_fwd{g}")
        dil_o.append(og), dil_l.append(lg)
    o_a = _dil_mix_fwd(dil_o, dil_l, name="dil_mix")

    c = _fox_cumsum(zf, p["b_fgt"], name="fox_cumsum")
    fqp, fkp, fvp = _fox_pack_fwd(qkv, c, name="fox_pack")
    o_b, flse = _fox_fwd(fqp, fkp, fvp, name="fox_fwd")

    if late_weights is not None:
        w = {**w, **late_weights(o_b)}
    y_a = _mm(o_a, w["dil_out"], name="y_a", tn=1024, out_dtype=_CD)
    y_b = _mm(o_b, w["fox_out"], name="y_b", tn=1024, out_dtype=_CD)
    merged, x1, h2 = _gated_mix_out(gl, p["b_gate"], y_a, y_b, w["out"], x, p["norm_ffn_g"], name="mix_out")
    gate, up, act = _ffn_in_act(h2, w["ffn_in"], name="ffn_in")
    loss, dx2, dg_final = _ffn_down_loss(act, w["ffn_down"], x1, p["norm_final_g"], tgt, name="ffn_down_loss")

    gw_ffn_down = _mm(act, dx2, name="gw_ffn_down", ta=True, out_dtype=_CD, tm=1408)
    dgu = _d_swiglu(dx2, w["ffn_down"], gate, up, name="d_swiglu")
    gw_ffn_in = _mm(h2, dgu, name="gw_ffn_in", ta=True, out_dtype=_CD, tn=1408, out_blocks=1408, b_halves=True)
    sink = grad_sink if grad_sink is not None else (lambda group, grads: None)
    tok = sink("ffn", dict(ffn_in=gw_ffn_in, ffn_down=gw_ffn_down))
    dx1, dg_ffn = _mm(dgu, w["ffn_in"], name="d_h2", tb=True, tk=1408, b_blocks=True, tm=1024, a_halves=True,
                      rms_bwd=(x1, p["norm_ffn_g"], dx2), after=tok)

    gw_out = _mm(merged, dx1, name="gw_out", ta=True, out_dtype=_CD)
    dy_a, dy_b, dgl, db_gate = _gate_bwd(dx1, w["out"], gl, p["b_gate"], y_a, y_b, name="gate_bwd")
    do_a = _mm(dy_a, w["dil_out"], name="d_o_a", tb=True)
    gw_dil_out = _mm(o_a, dy_a, name="gw_dil_out", ta=True, out_dtype=_CD, tn=1024)
    do_b = _mm(dy_b, w["fox_out"], name="d_o_b", tb=True)
    gw_fox_out = _mm(o_b, dy_b, name="gw_fox_out", ta=True, out_dtype=_CD, tn=1024)
    tok = sink("mix", dict(dil_out=gw_dil_out, fox_out=gw_fox_out, out=gw_out))

    bqp, bdop = _fox_pack_bwd(qkv, c, o_b, flse, do_b, name="fox_pack_bwd", after=tok)
    dqp, dkp, dvp, dck, dcq = _fox_bwd(bqp, fkp, fvp, bdop, name="fox_bwd")
    dc = dcq[:, 0, :].T - dck.reshape(S, N_FOX_HEADS, HEAD_DIM)[:, :, 0]
    dc = jnp.pad(dc, ((0, 0), (0, F_PAD - N_FOX_HEADS)))
    dzf, db_fgt = _fox_cumsum_bwd(dc, zf, p["b_fgt"], name="fox_cumsum_bwd")

    douts = _dil_mix_bwd(do_a, dil_o, dil_l, name="dil_mix_bwd", after=tok)
    dqs, dks, dvs = [], [], []
    for g in range(N_DIL_GROUPS):
        dq, dk, dv = _dil_bwd(dil_qkv[g], douts[3 + g], douts[g], g, name=f"dil_bwd{g}")
        for parts, t in ((dqs, dq), (dks, dk), (dvs, dv)):
            parts.extend([t[0].astype(_CD), t[1].astype(_CD)])
    dqkv = jnp.concatenate(dqs + dks + dvs + [dqp, dkp, dvp], axis=1)

    gw_qkv = _mm(h, dqkv, name="gw_qkv", ta=True, out_dtype=_CD, tn=768)
    gw_g = _mm(h, dgl, name="gw_gate", ta=True, out_dtype=_CD)
    gw_f = _mm(h, dzf, name="gw_f", ta=True, out_dtype=_CD)
    tok = sink("in", dict(qkv=gw_qkv, f=gw_f, g=gw_g))
    dh = _mm(dqkv, w["qkv"], name="d_h_qkv", tb=True, tk=1920, tm=2048, after=tok)
    dh = _mm(dgl, w["g"], name="d_h_gate", tb=True, add=dh)
    dx, dg_mix = _mm(dzf, w["f"], name="d_h_f", tb=True, add=dh, tm=512, rms_bwd=(x, p["norm_mix_g"], dx1))

    gw = dict(qkv=gw_qkv, f=gw_f, g=gw_g, dil_out=gw_dil_out, fox_out=gw_fox_out, out=gw_out, ffn_in=gw_ffn_in,
              ffn_down=gw_ffn_down)
    small = dict(norm_mix_g=dg_mix, b_fgt=db_fgt, b_gate=db_gate, norm_ffn_g=dg_ffn, norm_final_g=dg_final)
    return loss, dx, gw, small


def _position():
    return lax.axis_index("x"), lax.axis_index("y"), lax.axis_index("c")


def _other_chips(x, y):
    return [(1 - x, y), (x, 1 - y), (1 - x, 1 - y)]


ROW_TILE = 16


def _row_chunks(rows, want=4):
    n = want
    while n > 1 and rows % (n * ROW_TILE):
        n //= 2
    return n


SEM_SPEC = pl.BlockSpec(memory_space=pltpu.SEMAPHORE)
ANY_SPEC = pl.BlockSpec(memory_space=pl.ANY)
DATAFLOW = pltpu.SideEffectType.DATAFLOW_SIDE_EFFECTING


def _in_hbm(a):
    return pltpu.with_memory_space_constraint(a, pltpu.HBM)


def _split_copy_start(srcs, land_shapes, copies, after, *, name):
    n, m = len(srcs), len(land_shapes)

    def body(*refs):
        src_refs, land_refs = refs[:n], refs[n:n + m]
        send_sems, recv_sems = refs[n + m + 1], refs[n + m + 2]
        token = refs[-1]
        x, y, c = _position()
        for k, (src, dst, peer) in enumerate(copies(x, y, c, src_refs, land_refs)):
            pltpu.make_async_remote_copy(src_ref=src, dst_ref=dst, send_sem=send_sems.at[k], recv_sem=recv_sems.at[k],
                                         device_id=peer, device_id_type=MESH).start()
        token[...] = jnp.zeros_like(token)

    lands = [lax.empty(s.shape, s.dtype) for s in land_shapes]
    count = len(copies(0, 0, 0, srcs, lands))
    out = _pcall(
        body, name=name,
        out_shape=(pltpu.SemaphoreType.DMA((count,)), pltpu.SemaphoreType.DMA((count,)),
                   *[pltpu.HBM(s.shape, s.dtype) for s in srcs], *[pltpu.HBM(s.shape, s.dtype) for s in land_shapes],
                   jax.ShapeDtypeStruct((8, 128), F32)),
        in_specs=[HBM_SPEC] * (n + m) + [ANY_SPEC],
        out_specs=(SEM_SPEC, SEM_SPEC, *[HBM_SPEC] * (n + m), pl.BlockSpec(memory_space=pltpu.VMEM)),
        input_output_aliases={k: 2 + k for k in range(n + m)},
        compiler_params=pltpu.CompilerParams(has_side_effects=DATAFLOW),
    )(*[_in_hbm(s) for s in srcs], *[_in_hbm(l) for l in lands], after)
    return out[0], out[1], list(out[2:2 + n]), list(out[2 + n:2 + n + m]), out[-1]


def _split_copy_wait(send_sems, recv_sems, srcs, lands, copies, after, *, name):
    n, m = len(srcs), len(lands)

    def body(*refs):
        src_refs, land_refs = refs[:n], refs[n:n + m]
        send, recv = refs[n + m], refs[n + m + 1]
        x, y, c = _position()
        for k, (src, dst, peer) in enumerate(copies(x, y, c, src_refs, land_refs)):
            cp = pltpu.make_async_remote_copy(src_ref=src, dst_ref=dst, send_sem=send.at[k], recv_sem=recv.at[k],
                                              device_id=peer, device_id_type=MESH)
            cp.wait_send()
            cp.wait_recv()

    afters = list(after) if isinstance(after, (list, tuple)) else [after]
    out = _pcall(
        body, name=name,
        out_shape=tuple(pltpu.HBM(s.shape, s.dtype) for s in list(srcs) + list(lands)),
        in_specs=[HBM_SPEC] * (n + m) + [SEM_SPEC, SEM_SPEC] + [ANY_SPEC] * len(afters),
        out_specs=tuple([HBM_SPEC] * (n + m)),
        input_output_aliases={k: k for k in range(n + m)},
        compiler_params=pltpu.CompilerParams(has_side_effects=DATAFLOW),
    )(*srcs, *lands, send_sems, recv_sems, *afters)
    return list(out[:n]), list(out[n:])


def _gather_copies(x, y, c, shard_refs, land_refs):
    out = []
    for s, l in zip(shard_refs, land_refs):
        half = s.shape[0] // 2
        nq = _row_chunks(half)
        for cx, cy in _other_chips(x, y):
            for q in range(nq):
                rows = pl.ds(c * half + q * (half // nq), half // nq)
                out.append((s.at[rows, :], l.at[2 * x + y, rows, :], (cx, cy, c)))
    return out


def _gather_whole_copies(x, y, c, shard_refs, land_refs):
    out = []
    for s, l in zip(shard_refs, land_refs):
        nq = _row_chunks(s.shape[0])
        for cx, cy in _other_chips(x, y):
            for q in range(nq):
                rows = pl.ds(q * (s.shape[0] // nq), s.shape[0] // nq)
                out.append((s.at[rows, :], l.at[2 * x + y, rows, :], (cx, cy, c)))
    return out


def _scatter_all_copies(x, y, c, block_refs, land_refs):
    out = []
    for g, l in zip(block_refs, land_refs):
        half = g.shape[1] // 2
        nq = _row_chunks(half)
        size = half // nq
        for q in range(nq):
            rows = pl.ds((1 - c) * half + q * size, size)
            out.append((g.at[2 * x + y, rows, :], l.at[0, pl.ds(q * size, size), :], (x, y, 1 - c)))
        for r, (cx, cy) in enumerate(_other_chips(x, y)):
            for j in range(2):
                h = c if j == 0 else 1 - c
                for q in range(nq):
                    rows = pl.ds(h * half + q * size, size)
                    out.append((g.at[2 * cx + cy, rows, :], l.at[1 + 2 * r + j, pl.ds(q * size, size), :], (cx, cy, h)))
    return out


def _forward_halves(lands, *, name):
    n = len(lands)

    def body(*refs):
        ins = refs[:n]
        send_sems, recv_sems = refs[2 * n:]
        x, y, c = _position()
        copies = []
        for w in range(n):
            half = ins[w].shape[1] // 2
            for r, (cx, cy) in enumerate(_other_chips(x, y)):
                blk = ins[w].at[2 * cx + cy, pl.ds(c * half, half), :]
                cp = pltpu.make_async_remote_copy(src_ref=blk, dst_ref=blk, send_sem=send_sems.at[w, r],
                                                  recv_sem=recv_sems.at[w, r], device_id=(x, y, 1 - c),
                                                  device_id_type=MESH)
                cp.start()
                copies.append(cp)
        for w in range(n):
            half = ins[w].shape[1] // 2
            for r, (cx, cy) in enumerate(_other_chips(x, y)):
                blk = ins[w].at[2 * cx + cy, pl.ds((1 - c) * half, half), :]
                pltpu.make_async_remote_copy(src_ref=blk, dst_ref=blk, send_sem=send_sems.at[w, r],
                                             recv_sem=recv_sems.at[w, r], device_id=(x, y, 1 - c),
                                             device_id_type=MESH).wait_recv()
        for cp in copies:
            cp.wait_send()

    return _pcall(
        body, name=name, in_specs=[HBM_SPEC] * n, out_specs=[HBM_SPEC] * n,
        out_shape=[jax.ShapeDtypeStruct(l.shape, l.dtype) for l in lands],
        input_output_aliases={k: k for k in range(n)},
        scratch_shapes=[pltpu.SemaphoreType.DMA((n, 3)), pltpu.SemaphoreType.DMA((n, 3))],
    )(*lands)


def _share_halves(halves):
    n = len(halves)

    def body(*refs):
        ins, outs = refs[:n], refs[n:2 * n]
        send_sems, recv_sems = refs[2 * n:]
        x, y, c = _position()
        copies = []
        for w in range(n):
            cp = pltpu.make_async_remote_copy(src_ref=ins[w], dst_ref=outs[w], send_sem=send_sems.at[w],
                                              recv_sem=recv_sems.at[w], device_id=(x, y, 1 - c), device_id_type=MESH)
            cp.start()
            copies.append(cp)
        for cp in copies:
            cp.wait()

    return _pcall(
        body, name="share_halves", in_specs=[HBM_SPEC] * n, out_specs=[HBM_SPEC] * n,
        out_shape=[jax.ShapeDtypeStruct(h.shape, h.dtype) for h in halves],
        scratch_shapes=[pltpu.SemaphoreType.DMA((n,)), pltpu.SemaphoreType.DMA((n,))],
    )(*halves)


def _sum_small(part):
    rows, width = part.shape

    def body(x_ref, out_ref, all_ref, send_sems, recv_sems):
        x, y, c = _position()
        me, sibling = (x, y, c), (x, y, 1 - c)
        chips = _other_chips(x, y)

        def block(px, py, pc):
            return all_ref.at[pl.ds((4 * px + 2 * py + pc) * rows, rows), :]

        def copy(k, blk, to, src=None):
            return pltpu.make_async_remote_copy(
                src_ref=block(*blk) if src is None else src, dst_ref=block(*blk), send_sem=send_sems.at[k],
                recv_sem=recv_sems.at[k], device_id=to, device_id_type=MESH)

        all_ref[pl.ds((4 * x + 2 * y + c) * rows, rows), :] = x_ref[...]
        first = [copy(0, me, sibling, src=x_ref)]
        first += [copy(1 + j, me, (*chip, c), src=x_ref) for j, chip in enumerate(chips)]
        for cp in first:
            cp.start()
        passed = [copy(4 + j, (*chip, c), sibling) for j, chip in enumerate(chips)]
        for j, chip in enumerate(chips):
            copy(1 + j, (*chip, c), me).wait_recv()
            passed[j].start()
        copy(0, sibling, me).wait_recv()
        for j, chip in enumerate(chips):
            copy(4 + j, (*chip, 1 - c), me).wait_recv()
        for cp in first + passed:
            cp.wait_send()
        total = all_ref[0:rows, :]
        for d in range(1, 8):
            total = total + all_ref[d * rows:(d + 1) * rows, :]
        out_ref[...] = total

    vm = pl.BlockSpec(memory_space=pltpu.VMEM)
    return _pcall(
        body, name="sum_small", in_specs=[vm], out_specs=vm, out_shape=jax.ShapeDtypeStruct((rows, width), F32),
        scratch_shapes=[pltpu.VMEM((8 * rows, width), F32), pltpu.SemaphoreType.DMA((7,)), pltpu.SemaphoreType.DMA((7,))],
    )(part)


def _row_tile(R, C, itemsize=4, budget=1 << 20):
    for t in (512, 256, 128, 64, 32, 16, 8):
        if R % t == 0 and t * C * itemsize <= budget:
            return t
    return R


def _add_all(g, recv, where, *, name):
    _, R, C = g.shape
    half = R // 2
    t = _row_tile(half, C)
    nb = half // t

    def body(w_ref, g_ref, r_ref, o_ref):
        total = g_ref[0].astype(F32)
        for k in range(7):
            total = total + r_ref[k].astype(F32)
        o_ref[...] = total

    grid_spec = pltpu.PrefetchScalarGridSpec(
        num_scalar_prefetch=1, grid=(nb,),
        in_specs=[pl.BlockSpec((1, t, C), lambda i, wr: (wr[0], wr[1] * nb + i, 0)),
                  pl.BlockSpec((7, t, C), lambda i, wr: (0, i, 0))],
        out_specs=pl.BlockSpec((t, C), lambda i, wr: (i, 0)))
    return _pcall(body, name=name, grid_spec=grid_spec, out_shape=jax.ShapeDtypeStruct((half, C), F32),
                  compiler_params=_params("parallel"))(where, g, recv)


def _adamw(w, g, m, v, *, name):
    R, C = w.shape
    t = _row_tile(R, C)
    c1 = 1.0 - ADAM_B1 ** ADAM_STEP
    c2 = 1.0 - ADAM_B2 ** ADAM_STEP

    def body(w_ref, g_ref, m_ref, v_ref, d_ref, nm_ref, nv_ref):
        gv = g_ref[...]
        mn = ADAM_B1 * m_ref[...] + (1.0 - ADAM_B1) * gv
        vn = ADAM_B2 * v_ref[...] + (1.0 - ADAM_B2) * (gv * gv)
        d_ref[...] = -ADAM_LR * ((mn / c1) / (jnp.sqrt(vn / c2) + ADAM_EPS) + ADAM_WD * w_ref[...])
        nm_ref[...] = mn
        nv_ref[...] = vn

    blk = pl.BlockSpec((t, C), lambda i: (i, 0))
    shp = jax.ShapeDtypeStruct((R, C), F32)
    return _pcall(body, name=name, grid=(R // t,), in_specs=[blk] * 4, out_specs=[blk] * 3, out_shape=[shp] * 3,
                  compiler_params=_params("parallel"))(w, g, m, v)


def _adamw_halves(w, mine, theirs, m, v, core, *, name):
    R, C = w.shape
    half = R // 2
    t = _row_tile(half, C)
    nbh = half // t
    c1 = 1.0 - ADAM_B1 ** ADAM_STEP
    c2 = 1.0 - ADAM_B2 ** ADAM_STEP

    def body(core_ref, w_ref, a_ref, b_ref, m_ref, v_ref, g_ref, d_ref, nm_ref, nv_ref):
        gv = jnp.where(pl.program_id(0) // nbh == core_ref[0], a_ref[...], b_ref[...])
        mn = ADAM_B1 * m_ref[...] + (1.0 - ADAM_B1) * gv
        vn = ADAM_B2 * v_ref[...] + (1.0 - ADAM_B2) * (gv * gv)
        g_ref[...] = gv
        d_ref[...] = -ADAM_LR * ((mn / c1) / (jnp.sqrt(vn / c2) + ADAM_EPS) + ADAM_WD * w_ref[...])
        nm_ref[...] = mn
        nv_ref[...] = vn

    blk = pl.BlockSpec((t, C), lambda i, cr: (i, 0))
    hblk = pl.BlockSpec((t, C), lambda i, cr: (i % nbh, 0))
    shp = jax.ShapeDtypeStruct((R, C), F32)
    grid_spec = pltpu.PrefetchScalarGridSpec(num_scalar_prefetch=1, grid=(2 * nbh,),
                                             in_specs=[blk, hblk, hblk, blk, blk], out_specs=[blk] * 4)
    return _pcall(body, name=name, grid_spec=grid_spec, out_shape=[shp] * 4,
                  compiler_params=_params("parallel"))(core, w, mine, theirs, m, v)


BIG = ("w_in", "w_dil_out", "w_fox_out", "w_out", "w_ffn_in", "w_ffn_down")
SMALL = ("norm_mix_g", "b_fgt", "b_gate", "norm_ffn_g", "norm_final_g")
ORDER = ("norm_mix_g", "w_in", "b_fgt", "b_gate", "w_dil_out", "w_fox_out", "w_out", "norm_ffn_g", "w_ffn_in",
         "w_ffn_down", "norm_final_g")
SMALL_ROWS = {"norm_mix_g": (0, 1), "b_gate": (1, 3), "norm_ffn_g": (3, 4), "norm_final_g": (4, 5), "b_fgt": (5, 6)}


def _columns_to_blocks(full, ncol):
    K = full.shape[0]
    return full.reshape(K, 4, ncol).transpose(1, 0, 2)


def _blocks_to_columns(blocks):
    n, K, ncol = blocks.shape
    return blocks.transpose(1, 0, 2).reshape(K, n * ncol)


def kernel(x, norm_mix_g, w_in, b_fgt, b_gate, w_dil_out, w_fox_out, w_out, norm_ffn_g, w_ffn_in, w_ffn_down, norm_final_g, loss_target, m_norm_mix_g, m_w_in, m_b_fgt, m_b_gate, m_w_dil_out, m_w_fox_out, m_w_out, m_norm_ffn_g, m_w_ffn_in, m_w_ffn_down, m_norm_final_g, v_norm_mix_g, v_w_in, v_b_fgt, v_b_gate, v_w_dil_out, v_w_fox_out, v_w_out, v_norm_ffn_g, v_w_ffn_in, v_w_ffn_down, v_norm_final_g):
    weights = dict(norm_mix_g=norm_mix_g, w_in=w_in, b_fgt=b_fgt, b_gate=b_gate, w_dil_out=w_dil_out,
                   w_fox_out=w_fox_out, w_out=w_out, norm_ffn_g=norm_ffn_g, w_ffn_in=w_ffn_in, w_ffn_down=w_ffn_down,
                   norm_final_g=norm_final_g)
    m_in = dict(norm_mix_g=m_norm_mix_g, w_in=m_w_in, b_fgt=m_b_fgt, b_gate=m_b_gate, w_dil_out=m_w_dil_out,
                w_fox_out=m_w_fox_out, w_out=m_w_out, norm_ffn_g=m_norm_ffn_g, w_ffn_in=m_w_ffn_in,
                w_ffn_down=m_w_ffn_down, norm_final_g=m_norm_final_g)
    v_in = dict(norm_mix_g=v_norm_mix_g, w_in=v_w_in, b_fgt=v_b_fgt, b_gate=v_b_gate, w_dil_out=v_w_dil_out,
                w_fox_out=v_w_fox_out, w_out=v_w_out, norm_ffn_g=v_norm_ffn_g, w_ffn_in=v_w_ffn_in,
                w_ffn_down=v_w_ffn_down, norm_final_g=v_norm_final_g)
    c = lax.axis_index("c")
    chip = 2 * lax.axis_index("x") + lax.axis_index("y")

    shards = {n: weights[n][0].astype(_CD) for n in BIG}
    in_shape = jax.ShapeDtypeStruct((4,) + shards["w_in"].shape, _CD)
    send_i, recv_i, in_src, in_land, token_in = _split_copy_start(
        [shards["w_in"]], [in_shape], _gather_copies, norm_mix_g, name="gather_in_start")
    late = BIG[1:]
    send_g, recv_g, late_src, late_land, token = _split_copy_start(
        [shards[n] for n in late], [jax.ShapeDtypeStruct((4,) + shards[n].shape, _CD) for n in late],
        _gather_whole_copies, token_in, name="gather_late_start")
    adam_in = [t[0] + token_in[0, 0] for t in (w_in, m_w_in, v_w_in)]
    p = dict(norm_mix_g=norm_mix_g, b_fgt=jnp.pad(b_fgt, ((0, 0), (0, F_PAD - N_FOX_HEADS))), b_gate=b_gate,
             norm_ffn_g=norm_ffn_g, norm_final_g=norm_final_g.reshape(1, D_MODEL))

    def first_weights(after):
        own, lands = _split_copy_wait(send_i, recv_i, in_src, in_land, _gather_copies, [after] + adam_in,
                                      name="gather_in_wait")
        (g_in,) = _forward_halves(lands, name="gather_in_forward")
        full_in = _blocks_to_columns(lax.dynamic_update_index_in_dim(g_in, own[0], chip, 0))
        o3 = QKV_COLS
        o4 = o3 + N_FOX_HEADS
        return dict(qkv=full_in[:, :o3], f=jnp.pad(full_in[:, o3:o4], ((0, 0), (0, F_PAD - N_FOX_HEADS))),
                    g=full_in[:, o4:])

    def late_weights(after):
        own, lands = _split_copy_wait(send_g, recv_g, late_src, late_land, _gather_whole_copies, after,
                                      name="gather_late_wait")
        g_dil, g_fox, g_out, g_ffn_in, g_ffn_down = [
            lax.dynamic_update_index_in_dim(l, s, chip, 0) for l, s in zip(lands, own)]
        return dict(dil_out=_blocks_to_columns(g_dil), fox_out=_blocks_to_columns(g_fox),
                    out=g_out.reshape(D_MODEL, D_MODEL), ffn_in=g_ffn_in,
                    ffn_down=g_ffn_down.reshape(D_FF, D_MODEL))

    def to_blocks(n, full):
        shape = weights[n].shape
        if full.ndim == 3:
            return full
        if n in ("w_out", "w_ffn_down"):
            return full.reshape(4, shape[1], shape[2])
        return _columns_to_blocks(full, shape[2])

    in_flight = {}

    def grad_sink(group, gw):
        if group == "in":
            named = {"w_in": jnp.concatenate([gw["qkv"], gw["f"][:, :N_FOX_HEADS], gw["g"]], axis=1)}
        else:
            named = {"w_" + k: v for k, v in gw.items()}
        srcs = [to_blocks(n, named[n]) for n in named]
        lands = [jax.ShapeDtypeStruct((7, s.shape[1] // 2, s.shape[2]), s.dtype) for s in srcs]
        started = _split_copy_start(srcs, lands, _scatter_all_copies, next(iter(gw.values())),
                                    name=f"scatter_{group}_start")
        in_flight[group] = (list(named), started)
        return started[-1]

    loss_part, grad_x, gw, small = _layer_step(x[0], loss_target[0], {}, p, late_weights, grad_sink,
                                               (token_in, token), first_weights)

    halves = {}
    where = jnp.stack([chip, c]).astype(jnp.int32)
    for group, (names, (send_s, recv_s, srcs, lands, _)) in in_flight.items():
        srcs, recv = _split_copy_wait(send_s, recv_s, srcs, lands, _scatter_all_copies, grad_x,
                                      name=f"scatter_{group}_wait")
        halves.update({n: _add_all(s, r, where, name=f"add_all_{n}") for n, s, r in zip(names, srcs, recv)})
    halves = [halves[n] for n in BIG]
    grad_halves = dict(zip(BIG, zip(halves, _share_halves(halves))))

    packed = jnp.concatenate([
        small["norm_mix_g"], small["b_gate"].reshape(2, D_MODEL), small["norm_ffn_g"], small["norm_final_g"],
        jnp.pad(small["b_fgt"], ((0, 0), (0, D_MODEL - F_PAD))), jnp.pad(loss_part, ((0, 0), (0, D_MODEL - 1))),
        jnp.zeros((1, D_MODEL), F32)], axis=0)
    summed = _sum_small(packed)
    loss = summed[6, 0]

    out_g, out_d, out_m, out_v = {}, {}, {}, {}
    core = jnp.reshape(c, (1,)).astype(jnp.int32)
    for n in SMALL:
        lo, hi = SMALL_ROWS[n]
        shape = weights[n].shape
        g2 = summed[lo:hi].reshape(1, -1)[:, :weights[n].size]
        d2, m2, v2 = _adamw(weights[n].reshape(g2.shape), g2, m_in[n].reshape(g2.shape), v_in[n].reshape(g2.shape),
                            name=f"adamw_{n}")
        out_g[n], out_d[n], out_m[n], out_v[n] = [t.reshape(shape) for t in (g2, d2, m2, v2)]
    for n in BIG:
        shape = weights[n].shape
        wmv = adam_in if n == "w_in" else [t[0] for t in (weights[n], m_in[n], v_in[n])]
        mine, theirs = grad_halves[n]
        outs = _adamw_halves(wmv[0], mine, theirs, wmv[1], wmv[2], core, name=f"adamw_{n}")
        out_g[n], out_d[n], out_m[n], out_v[n] = [t.reshape(shape) for t in outs]
    return (loss, grad_x[None], *[out_g[n] for n in ORDER], *[out_d[n] for n in ORDER],
            *[out_m[n] for n in ORDER], *[out_v[n] for n in ORDER])
```

```python
import numpy as np
import jax
import jax.numpy as jnp
from jax import lax
from jax.experimental import pallas as pl
from jax.experimental.pallas import tpu as pltpu

F32 = jnp.float32
_CD = jnp.bfloat16

D_MODEL = 1024
HEAD_DIM = 64
DIL_PAIRS = ((128, 1), (512, 4), (2048, 16))
N_DIL_GROUPS = 3
DIL_HEADS = 4
DIL_W = 128
DIL_OUT = DIL_HEADS * HEAD_DIM
DIL_WIDTH = N_DIL_GROUPS * DIL_OUT
N_FOX_HEADS = 8
FOX_WIDTH = N_FOX_HEADS * HEAD_DIM
D_FF = 2816
QKV_COLS = 3 * DIL_WIDTH + 3 * FOX_WIDTH
F_PAD = 128
RMS_EPS = 1e-6
NEG_INF = -1e30
ATTN_SCALE = HEAD_DIM ** -0.5
ADAM_LR, ADAM_B1, ADAM_B2, ADAM_EPS, ADAM_WD, ADAM_STEP = 0.001, 0.9, 0.999, 1e-08, 0.01, 10

VMEM_LIMIT = 48 * 1024 * 1024
VMEM_LIMIT_RESIDENT = 56 * 1024 * 1024
LANES = 128
MESH = pl.DeviceIdType.MESH
HBM_SPEC = pl.BlockSpec(memory_space=pltpu.HBM)


def _pcall(body, after=None, **kw):
    if after is None:
        return pl.pallas_call(body, **kw)
    n_in = len(kw["in_specs"])
    kw["in_specs"] = list(kw["in_specs"]) + [pl.BlockSpec(memory_space=pl.ANY)]

    def tied(*refs):
        return body(*refs[:n_in], *refs[n_in + 1:])

    call = pl.pallas_call(tied, **kw)
    return lambda *args: call(*args, after)


def _params(*sem):
    return pltpu.CompilerParams(dimension_semantics=sem, vmem_limit_bytes=VMEM_LIMIT)


def _pick(dim, pref):
    t = (min(pref, dim) // 128) * 128
    while t >= 128:
        if dim % t == 0:
            return t
        t -= 128
    return dim


def _mm(a, b, *, name, ta=False, tb=False, out_dtype=F32, add=None, tm=1024, tn=512, tk=2048, after=None,
        b_blocks=False, out_blocks=None, a_halves=False, b_halves=False, rms_bwd=None, more=()):
    if a_halves:
        M, K = a.shape[1], 2 * a.shape[2]
    elif ta:
        K, M = a.shape
    else:
        M, K = a.shape
    if b_halves:
        b_rows, b_cols = b.shape[1], 2 * b.shape[2]
    else:
        b_rows, b_cols = (b.shape[1], b.shape[0] * b.shape[2]) if b_blocks else b.shape
    if tb:
        N, K2 = b_rows, b_cols
    else:
        K2, N = b_rows, b_cols
    assert K == K2, (a.shape, b.shape)
    shard = b.shape[2] if b_blocks else None
    tm = _pick(M, tm)
    tn = _pick(shard if (b_blocks and not tb) else (out_blocks or N), tn)
    tk = _pick(shard if (b_blocks and tb) else K, tk)
    nk = K // tk
    dn = (((0 if ta else 1,), (1 if tb else 0,)), ((), ()))
    has_add = add is not None
    assert not (has_add and out_blocks)
    has_norm = rms_bwd is not None
    if has_norm:
        tn = N
        assert not out_blocks and out_dtype == F32
    assert not more or (nk == 1 and tb and not ta)

    def body(*refs):
        a_ref, b_ref = refs[0], refs[1]
        rest = list(refs[2:])
        more_refs = [(rest.pop(0), rest.pop(0)) for _ in more]
        add_ref = rest.pop(0) if has_add else None
        x_ref, g_ref, dres_ref = (rest.pop(0), rest.pop(0), rest.pop(0)) if has_norm else (None, None, None)
        o_ref = rest.pop(0)
        dg_ref = rest.pop(0) if has_norm else None
        bv = b_ref[0] if b_blocks else b_ref[...]
        p = lax.dot_general(a_ref[...].astype(_CD), bv.astype(_CD), dn, preferred_element_type=F32)
        for a2_ref, b2_ref in more_refs:
            p += lax.dot_general(a2_ref[...].astype(_CD), b2_ref[...].astype(_CD), dn, preferred_element_type=F32)

        def finish(r):
            if has_add:
                r = r + add_ref[...]
            if has_norm:
                xv = x_ref[...]
                rs = lax.rsqrt(jnp.mean(xv * xv, axis=-1, keepdims=True) + RMS_EPS)
                xh = xv * rs
                dxh = r * g_ref[...]
                o_ref[...] = dres_ref[...] + rs * (dxh - xh * jnp.mean(dxh * xh, axis=-1, keepdims=True))
                part = jnp.sum(r * xh, axis=0, keepdims=True)
                first = pl.program_id(0) == 0

                @pl.when(first)
                def _():
                    dg_ref[...] = part

                @pl.when(jnp.logical_not(first))
                def _():
                    dg_ref[...] += part
            elif out_blocks:
                o_ref[0] = r.astype(out_dtype)
            else:
                o_ref[...] = r.astype(out_dtype)

        if nk == 1:
            finish(p)
        else:
            acc_ref = rest.pop(0)
            k = pl.program_id(2)

            @pl.when(k == 0)
            def _():
                acc_ref[...] = p

            @pl.when(k > 0)
            def _():
                acc_ref[...] += p

            @pl.when(k == nk - 1)
            def _():
                finish(acc_ref[...])

    if a_halves:
        ka = (K // 2) // tk
        a_spec = pl.BlockSpec((None, tm, tk), lambda i, j, k: (k // ka, i, k % ka))
    else:
        a_spec = pl.BlockSpec((tk, tm), lambda i, j, k: (k, i)) if ta else pl.BlockSpec((tm, tk), lambda i, j, k: (i, k))
    if b_halves:
        nb_ = (N // 2) // tn
        b_spec = pl.BlockSpec((None, tk, tn), lambda i, j, k: (j // nb_, k, j % nb_))
    elif b_blocks and tb:
        per = shard // tk
        b_spec = pl.BlockSpec((1, tn, tk), lambda i, j, k: (k // per, j, k % per))
    elif b_blocks:
        per = shard // tn
        b_spec = pl.BlockSpec((1, tk, tn), lambda i, j, k: (j // per, k, j % per))
    else:
        b_spec = pl.BlockSpec((tn, tk), lambda i, j, k: (j, k)) if tb else pl.BlockSpec((tk, tn), lambda i, j, k: (k, j))
    if out_blocks:
        oper = out_blocks // tn
        o_spec = pl.BlockSpec((1, tm, tn), lambda i, j, k: (j // oper, i, j % oper))
        out_shape = jax.ShapeDtypeStruct((N // out_blocks, M, out_blocks), out_dtype)
    else:
        o_spec = pl.BlockSpec((tm, tn), lambda i, j, k: (i, j))
        out_shape = jax.ShapeDtypeStruct((M, N), out_dtype)
    in_specs, args = [a_spec, b_spec], (a, b)
    for a2, b2 in more:
        assert a2.shape[0] == M and b2.shape == (N, a2.shape[1]), (a2.shape, b2.shape)
        in_specs += [pl.BlockSpec((tm, a2.shape[1]), lambda i, j, k: (i, 0)),
                     pl.BlockSpec((tn, a2.shape[1]), lambda i, j, k: (j, 0))]
        args += (a2, b2)
    if has_add:
        in_specs, args = in_specs + [o_spec], args + (add,)
    out_specs, semantics = o_spec, ("parallel", "parallel", "arbitrary")
    if has_norm:
        vec = pl.BlockSpec((1, N), lambda i, j, k: (0, 0))
        in_specs += [o_spec, vec, o_spec]
        args += tuple(rms_bwd)
        out_specs, out_shape = [o_spec, vec], [out_shape, jax.ShapeDtypeStruct((1, N), F32)]
        semantics = ("arbitrary", "arbitrary", "arbitrary")
    return _pcall(
        body, after, name=name, grid=(M // tm, N // tn, nk), in_specs=in_specs, out_specs=out_specs,
        out_shape=out_shape,
        scratch_shapes=[pltpu.VMEM((tm, tn), F32)] if nk > 1 else [],
        compiler_params=_params(*semantics),
    )(*args)


def _rms_fwd(x, g, *, name, tm=512, after=None):
    S, D = x.shape

    def body(x_ref, g_ref, h_ref):
        xv = x_ref[...]
        r = lax.rsqrt(jnp.mean(xv * xv, axis=-1, keepdims=True) + RMS_EPS)
        h_ref[...] = ((xv * r) * g_ref[...]).astype(h_ref.dtype)

    row = pl.BlockSpec((tm, D), lambda i: (i, 0))
    return _pcall(body, after, name=name, grid=(S // tm,), in_specs=[row, pl.BlockSpec((1, D), lambda i: (0, 0))],
                  out_specs=row, out_shape=jax.ShapeDtypeStruct((S, D), _CD), compiler_params=_params("parallel"))(x, g)


def _ffn_down_loss(act, w_down, x1, g, tgt, *, name, tm=512):
    S, D = x1.shape
    F = act.shape[1]

    def body(a_ref, b_ref, x_ref, g_ref, t_ref, loss_ref, dx_ref, dg_ref):
        xv = x_ref[...] + jnp.dot(a_ref[...].astype(_CD), b_ref[...].astype(_CD), preferred_element_type=F32)
        gv = g_ref[...]
        r = lax.rsqrt(jnp.mean(xv * xv, axis=-1, keepdims=True) + RMS_EPS)
        xh = xv * r
        err = xh * gv - t_ref[...]
        lpart = 0.5 * jnp.sum(jnp.mean(err * err, axis=-1, keepdims=True), axis=0, keepdims=True)
        dy = err * (1.0 / D)
        dxh = dy * gv
        dx_ref[...] = r * (dxh - xh * jnp.mean(dxh * xh, axis=-1, keepdims=True))
        gpart = jnp.sum(dy * xh, axis=0, keepdims=True)

        @pl.when(pl.program_id(0) == 0)
        def _():
            loss_ref[...] = lpart
            dg_ref[...] = gpart

        @pl.when(pl.program_id(0) > 0)
        def _():
            loss_ref[...] += lpart
            dg_ref[...] += gpart

    row = pl.BlockSpec((tm, D), lambda i: (i, 0))
    vec = pl.BlockSpec((1, D), lambda i: (0, 0))
    one = pl.BlockSpec((1, 1), lambda i: (0, 0))
    return _pcall(body, name=name, grid=(S // tm,),
                  in_specs=[pl.BlockSpec((tm, F), lambda i: (i, 0)), pl.BlockSpec((F, D), lambda i: (0, 0)), row, vec, row],
                  out_specs=[one, row, vec],
                  out_shape=[jax.ShapeDtypeStruct((1, 1), F32), jax.ShapeDtypeStruct((S, D), F32),
                             jax.ShapeDtypeStruct((1, D), F32)],
                  compiler_params=_params("arbitrary"))(act, w_down, x1, g, tgt)


def _sigmoid(z):
    return 1.0 / (1.0 + jnp.exp(-z))


def _gated_mix_out(gl, bg, ya, yb, w_out, x, g, *, name, tm=512):
    S, D = ya.shape

    def body(za_ref, zb_ref, ba_ref, bb_ref, ya_ref, yb_ref, w_ref, x_ref, g_ref, m_ref, x1_ref, h_ref):
        ga = _sigmoid(za_ref[...].astype(F32) + ba_ref[...])
        gb = _sigmoid(zb_ref[...].astype(F32) + bb_ref[...])
        merged = (ga * ya_ref[...].astype(F32) + gb * yb_ref[...].astype(F32)).astype(m_ref.dtype)
        m_ref[...] = merged
        x1 = x_ref[...] + jnp.dot(merged, w_ref[...].astype(_CD), preferred_element_type=F32)
        x1_ref[...] = x1
        rs = lax.rsqrt(jnp.mean(x1 * x1, axis=-1, keepdims=True) + RMS_EPS)
        h_ref[...] = ((x1 * rs) * g_ref[...]).astype(h_ref.dtype)

    lo = pl.BlockSpec((tm, D), lambda i: (i, 0))
    hi = pl.BlockSpec((tm, D), lambda i: (i, 1))
    vlo = pl.BlockSpec((1, D), lambda i: (0, 0))
    vhi = pl.BlockSpec((1, D), lambda i: (0, 1))
    whole = pl.BlockSpec((D, D), lambda i: (0, 0))
    return _pcall(body, name=name, grid=(S // tm,), in_specs=[lo, hi, vlo, vhi, lo, lo, whole, lo, vlo],
                  out_specs=[lo, lo, lo],
                  out_shape=[jax.ShapeDtypeStruct((S, D), _CD), jax.ShapeDtypeStruct((S, D), F32),
                             jax.ShapeDtypeStruct((S, D), _CD)],
                  compiler_params=_params("parallel"))(gl, gl, bg, bg, ya, yb, w_out, x, g)


def _gate_bwd(dx1, w_out, gl, bg, ya, yb, *, name, tm=512):
    S, D = ya.shape
    nt = (((1,), (1,)), ((), ()))

    def body(dx_ref, w_ref, za_ref, zb_ref, ba_ref, bb_ref, ya_ref, yb_ref, dya_ref, dyb_ref, dgl_ref, dbg_ref):
        dmv = lax.dot_general(dx_ref[...].astype(_CD), w_ref[...].astype(_CD), nt, preferred_element_type=F32)
        ga = _sigmoid(za_ref[...].astype(F32) + ba_ref[...])
        gb = _sigmoid(zb_ref[...].astype(F32) + bb_ref[...])
        dya_ref[...] = (dmv * ga).astype(dya_ref.dtype)
        dyb_ref[...] = (dmv * gb).astype(dyb_ref.dtype)
        dza = dmv * ya_ref[...].astype(F32) * ga * (1.0 - ga)
        dzb = dmv * yb_ref[...].astype(F32) * gb * (1.0 - gb)
        dgl_ref[:, :D] = dza.astype(dgl_ref.dtype)
        dgl_ref[:, D:] = dzb.astype(dgl_ref.dtype)
        pa = jnp.sum(dza, axis=0, keepdims=True)
        pb = jnp.sum(dzb, axis=0, keepdims=True)

        @pl.when(pl.program_id(0) == 0)
        def _():
            dbg_ref[:, :D] = pa
            dbg_ref[:, D:] = pb

        @pl.when(pl.program_id(0) > 0)
        def _():
            dbg_ref[:, :D] += pa
            dbg_ref[:, D:] += pb

    lo = pl.BlockSpec((tm, D), lambda i: (i, 0))
    hi = pl.BlockSpec((tm, D), lambda i: (i, 1))
    vlo = pl.BlockSpec((1, D), lambda i: (0, 0))
    vhi = pl.BlockSpec((1, D), lambda i: (0, 1))
    wide = pl.BlockSpec((tm, 2 * D), lambda i: (i, 0))
    vwide = pl.BlockSpec((1, 2 * D), lambda i: (0, 0))
    whole = pl.BlockSpec((D, D), lambda i: (0, 0))
    return _pcall(body, name=name, grid=(S // tm,), in_specs=[lo, whole, lo, hi, vlo, vhi, lo, lo],
                  out_specs=[lo, lo, wide, vwide],
                  out_shape=[jax.ShapeDtypeStruct((S, D), _CD), jax.ShapeDtypeStruct((S, D), _CD),
                             jax.ShapeDtypeStruct((S, 2 * D), _CD), jax.ShapeDtypeStruct((1, 2 * D), F32)],
                  compiler_params=_params("arbitrary"))(dx1, w_out, gl, gl, bg, bg, ya, yb)


def _ffn_in_act(h2, w_blocks, *, name, tm=512):
    S, D = h2.shape
    _, _, C = w_blocks.shape

    def body(a_ref, bg_ref, bu_ref, g_ref, u_ref, o_ref):
        av = a_ref[...].astype(_CD)
        gv = jnp.dot(av, bg_ref[0].astype(_CD), preferred_element_type=F32)
        uv = jnp.dot(av, bu_ref[0].astype(_CD), preferred_element_type=F32)
        g_ref[...] = gv.astype(g_ref.dtype)
        u_ref[...] = uv.astype(u_ref.dtype)
        o_ref[...] = (gv * _sigmoid(gv) * uv).astype(o_ref.dtype)

    out = pl.BlockSpec((tm, C), lambda i, j: (i, j))
    shp = jax.ShapeDtypeStruct((S, 2 * C), _CD)
    return _pcall(body, name=name, grid=(S // tm, 2),
                  in_specs=[pl.BlockSpec((tm, D), lambda i, j: (i, 0)), pl.BlockSpec((1, D, C), lambda i, j: (j, 0, 0)),
                            pl.BlockSpec((1, D, C), lambda i, j: (2 + j, 0, 0))],
                  out_specs=[out, out, out], out_shape=[shp, shp, shp],
                  compiler_params=_params("parallel", "arbitrary"))(h2, w_blocks, w_blocks)


def _d_swiglu(dx, w_down, gate, up, *, name, tm=512, tn=1408):
    S, D = dx.shape
    F = w_down.shape[0]
    nt = (((1,), (1,)), ((), ()))

    def body(a_ref, b_ref, g_ref, u_ref, o_ref):
        dv = lax.dot_general(a_ref[...].astype(_CD), b_ref[...].astype(_CD), nt, preferred_element_type=F32)
        gv = g_ref[...].astype(F32)
        sg = _sigmoid(gv)
        o_ref[0] = (dv * u_ref[...].astype(F32) * (sg * (1.0 + gv * (1.0 - sg)))).astype(o_ref.dtype)
        o_ref[1] = (dv * (gv * sg)).astype(o_ref.dtype)

    tile = pl.BlockSpec((tm, tn), lambda i, j: (i, j))
    return _pcall(body, name=name, grid=(S // tm, F // tn),
                  in_specs=[pl.BlockSpec((tm, D), lambda i, j: (i, 0)), pl.BlockSpec((tn, D), lambda i, j: (j, 0)),
                            tile, tile],
                  out_specs=pl.BlockSpec((2, tm, tn), lambda i, j: (0, i, j)),
                  out_shape=jax.ShapeDtypeStruct((2, S, F), _CD),
                  compiler_params=_params("parallel", "arbitrary"))(dx, w_down, gate, up)


def _split3(x):
    hi = x.astype(jnp.bfloat16)
    r1 = x - hi.astype(F32)
    mid = r1.astype(jnp.bfloat16)
    lo = (r1 - mid.astype(F32)).astype(jnp.bfloat16)
    return hi, mid, lo


def _ones_dot_left(ones, x):
    return sum(jnp.dot(ones, p, preferred_element_type=F32) for p in _split3(x))


def _ones_dot_right(x, ones):
    return sum(jnp.dot(p, ones, preferred_element_type=F32) for p in _split3(x))


def _head_sum(x):
    n = x.shape[1]
    r = lax.broadcasted_iota(jnp.int32, (n, n), 0) // HEAD_DIM
    c = lax.broadcasted_iota(jnp.int32, (n, n), 1) // HEAD_DIM
    return _ones_dot_right(x, (r == c).astype(jnp.bfloat16))


def _log_sigmoid(z):
    e = jnp.exp(-jnp.abs(z))
    t = 1.0 + e
    log1p_e = jnp.where(t == 1.0, e, jnp.log(t) * (e / jnp.where(t == 1.0, 1.0, t - 1.0)))
    return jnp.minimum(z, 0.0) - log1p_e


def _fox_cumsum(zf, bf, *, name):
    S, W = zf.shape
    nb = S // 128

    def body(z_ref, b_ref, c_ref):
        tri = (lax.broadcasted_iota(jnp.int32, (128, 128), 0) >= lax.broadcasted_iota(jnp.int32, (128, 128), 1))
        tri = tri.astype(jnp.bfloat16)

        def step(i, carry):
            rows = pl.ds(pl.multiple_of(i * 128, 128), 128)
            lf = _log_sigmoid(z_ref[rows, :] + b_ref[...])
            cb = _ones_dot_left(tri, lf) + carry
            c_ref[rows, :] = cb
            return cb[127:128, :]

        lax.fori_loop(0, nb, step, jnp.zeros((1, W), F32))

    return _pcall(body, name=name, out_shape=jax.ShapeDtypeStruct((S, W), F32),
                  compiler_params=pltpu.CompilerParams(vmem_limit_bytes=VMEM_LIMIT))(zf, bf)


def _fox_cumsum_bwd(dc, zf, bf, *, name):
    S, W = zf.shape
    nb = S // 128

    def body(dc_ref, z_ref, b_ref, dz_ref, db_ref):
        tri = (lax.broadcasted_iota(jnp.int32, (128, 128), 0) <= lax.broadcasted_iota(jnp.int32, (128, 128), 1))
        tri = tri.astype(jnp.bfloat16)

        def step(k, carry):
            tail, acc = carry
            i = nb - 1 - k
            rows = pl.ds(pl.multiple_of(i * 128, 128), 128)
            dlf = _ones_dot_left(tri, dc_ref[rows, :]) + tail
            dz = dlf * _sigmoid(-(z_ref[rows, :] + b_ref[...]))
            dz_ref[rows, :] = dz
            return dlf[0:1, :], acc + jnp.sum(dz, axis=0, keepdims=True)

        _, acc = lax.fori_loop(0, nb, step, (jnp.zeros((1, W), F32), jnp.zeros((1, W), F32)))
        db_ref[...] = acc

    return _pcall(body, name=name,
                  out_shape=[jax.ShapeDtypeStruct((S, W), F32), jax.ShapeDtypeStruct((1, W), F32)],
                  compiler_params=pltpu.CompilerParams(vmem_limit_bytes=VMEM_LIMIT))(dc, zf, bf)


def _proj_dil(h, w_qkv, *, name, tm=1024):
    S, D = h.shape
    tn = DIL_WIDTH

    def body(a_ref, b_ref, *rest):
        outs, acc = rest[:N_DIL_GROUPS], rest[N_DIL_GROUPS]
        prod = jnp.dot(a_ref[...].astype(_CD), b_ref[...].astype(_CD), preferred_element_type=F32)
        for k in range(tn // LANES):
            acc[k] = prod[:, k * LANES:(k + 1) * LANES]
        for g, (_, d) in enumerate(DIL_PAIRS):
            for half in range(DIL_OUT // LANES):
                k = g * (DIL_OUT // LANES) + half
                cols = slice(half * LANES, (half + 1) * LANES)
                for r in range(d):
                    rows = pl.ds(r, tm // d, stride=d) if d > 1 else slice(None)
                    outs[g][0, r, :, cols] = acc[k, rows, :].astype(outs[g].dtype)

    out_specs = [pl.BlockSpec((1, d, tm // d, DIL_OUT), lambda i, j: (j, 0, i, 0)) for _, d in DIL_PAIRS]
    out_shape = [jax.ShapeDtypeStruct((3, d, S // d, DIL_OUT), _CD) for _, d in DIL_PAIRS]
    outs = _pcall(body, name=name, grid=(S // tm, 3),
                  in_specs=[pl.BlockSpec((tm, D), lambda i, j: (i, 0)), pl.BlockSpec((D, tn), lambda i, j: (0, j))],
                  out_specs=out_specs, out_shape=out_shape, scratch_shapes=[pltpu.VMEM((tn // LANES, tm, LANES), F32)],
                  compiler_params=_params("parallel", "arbitrary"))(h, w_qkv)
    return [o.reshape(3, S, DIL_OUT) for o in outs]


def _dil_start(block, S, dilation):
    sub = S // dilation
    u0 = block * DIL_W
    return (u0 % sub) * dilation + u0 // sub


def _dil_slopes(group):
    h = np.arange(1, N_DIL_GROUPS * DIL_HEADS + 1, dtype=np.float32)
    s = (np.float32(2.0) ** (np.float32(-8.0) * h / np.float32(N_DIL_GROUPS * DIL_HEADS))).astype(np.float32)
    return [float(v) for v in s.reshape(N_DIL_GROUPS, DIL_HEADS)[group]]


def _dil_tiles(i, n, blocks_per_seq):
    qi = lax.broadcasted_iota(jnp.int32, (DIL_W, 2 * DIL_W), 0)
    kj = lax.broadcasted_iota(jnp.int32, (DIL_W, 2 * DIL_W), 1)
    rel = qi + DIL_W - kj
    first = ((4 * n + i) % blocks_per_seq) == 0
    valid = jnp.logical_and(jnp.logical_and(rel >= 0, rel <= DIL_W), jnp.logical_or(kj >= DIL_W, jnp.logical_not(first)))
    return valid, rel.astype(F32)


def _dil_window(cur_ref, prev_ref, i, cols):
    if i > 0:
        return cur_ref[(i - 1) * DIL_W:(i + 1) * DIL_W, cols]
    return jnp.concatenate([prev_ref[:, cols], cur_ref[:DIL_W, cols]], axis=0)


CHUNK = 4 * DIL_W


def _dil_rows(block, S, dilation):
    start = _dil_start(block, S, dilation)
    return pl.ds(start, DIL_W, stride=dilation) if dilation > 1 else pl.ds(start, DIL_W)


def SPLIT(S):
    return (DIL_OUT // LANES, S, LANES)


def _dil_fwd(qkv, group, *, name):
    S = qkv.shape[1]
    dilation = DIL_PAIRS[group][1]
    bps = (S // dilation) // DIL_W
    slopes = _dil_slopes(group)
    nt = (((1,), (1,)), ((), ()))

    def body(q_ref, k_ref, v_ref, kp_ref, vp_ref, on_ref, ln_ref, o_ref, l_ref):
        n = pl.program_id(0)
        for i in range(4):
            valid, rel = _dil_tiles(i, n, bps)
            rows = slice(i * DIL_W, (i + 1) * DIL_W)
            for h in range(DIL_HEADS):
                cols = slice(h * HEAD_DIM, (h + 1) * HEAD_DIM)
                qh = q_ref[rows, cols]
                k2, v2 = _dil_window(k_ref, kp_ref, i, cols), _dil_window(v_ref, vp_ref, i, cols)
                s = lax.dot_general(qh, k2, nt, preferred_element_type=F32) * ATTN_SCALE - (slopes[h] * dilation) * rel
                s = jnp.where(valid, s, NEG_INF)
                m = jnp.max(s, axis=-1, keepdims=True)
                p = jnp.exp(s - m)
                den = jnp.sum(p, axis=-1, keepdims=True)
                acc = jnp.dot(p.astype(_CD), v2, preferred_element_type=F32)
                o_ref[rows, cols] = acc / den
                l_ref[rows, cols] = jnp.broadcast_to(m + jnp.log(den), (DIL_W, HEAD_DIM))
        for i in range(4):
            rows = slice(i * DIL_W, (i + 1) * DIL_W)
            nat = _dil_rows(4 * n + i, S, dilation)
            for half in range(DIL_OUT // LANES):
                cols = slice(half * LANES, (half + 1) * LANES)
                on_ref[half, nat, :] = o_ref[rows, cols]
                ln_ref[half, nat, :] = l_ref[rows, cols]

    def cur(which):
        return pl.BlockSpec((None, CHUNK, DIL_OUT), lambda n: (which, n, 0))

    def prev(which):
        return pl.BlockSpec((None, DIL_W, DIL_OUT), lambda n: (which, jnp.maximum(4 * n - 1, 0), 0))

    whole = pl.BlockSpec(SPLIT(S), lambda n: (0, 0, 0))
    return _pcall(body, name=name, grid=(S // CHUNK,), in_specs=[cur(0), cur(1), cur(2), prev(1), prev(2)],
                  out_specs=[whole, whole],
                  out_shape=[jax.ShapeDtypeStruct(SPLIT(S), F32), jax.ShapeDtypeStruct(SPLIT(S), F32)],
                  scratch_shapes=[pltpu.VMEM((CHUNK, DIL_OUT), F32), pltpu.VMEM((CHUNK, DIL_OUT), F32)],
                  compiler_params=_params("arbitrary"))(qkv, qkv, qkv, qkv, qkv)


STAT_OFFSET = HEAD_DIM // 2


def _dil_bwd(qkv, stats, do, group, *, name):
    S = qkv.shape[1]
    dilation = DIL_PAIRS[group][1]
    bps = (S // dilation) // DIL_W
    slopes = _dil_slopes(group)
    nchunk = S // CHUNK
    nt = (((1,), (1,)), ((), ()))
    tn = (((0,), (0,)), ((), ()))

    def body(q_ref, k_ref, v_ref, kp_ref, vp_ref, ln_ref, don_ref, dqn_ref, dkn_ref, dvn_ref,
             dk_s, dv_s, l_ref, do_ref, dq_ref):
        step = pl.program_id(0)
        n = nchunk - 1 - step
        for i in range(4):
            rows = slice(i * DIL_W, (i + 1) * DIL_W)
            nat = _dil_rows(4 * n + i, S, dilation)
            for half in range(DIL_OUT // LANES):
                cols = slice(half * LANES, (half + 1) * LANES)
                l_ref[rows, cols] = ln_ref[half, nat, :]
                do_ref[rows, cols] = don_ref[half, nat, :]

        @pl.when(step == 0)
        def _():
            dk_s[:, CHUNK:] = jnp.zeros((DIL_OUT, DIL_W), F32)
            dv_s[:, CHUNK:] = jnp.zeros((DIL_OUT, DIL_W), F32)

        dk_s[:, :CHUNK] = jnp.zeros((DIL_OUT, CHUNK), F32)
        dv_s[:, :CHUNK] = jnp.zeros((DIL_OUT, CHUNK), F32)
        for i in range(4):
            valid, rel = _dil_tiles(i, n, bps)
            rows = slice(i * DIL_W, (i + 1) * DIL_W)
            window = slice(i * DIL_W, (i + 2) * DIL_W)
            for h in range(DIL_HEADS):
                cols = slice(h * HEAD_DIM, (h + 1) * HEAD_DIM)
                qh = q_ref[rows, cols]
                k2, v2 = _dil_window(k_ref, kp_ref, i, cols), _dil_window(v_ref, vp_ref, i, cols)
                lh = l_ref[rows, h * HEAD_DIM:h * HEAD_DIM + 1]
                shift = l_ref[rows, h * HEAD_DIM + STAT_OFFSET:h * HEAD_DIM + STAT_OFFSET + 1]
                s = lax.dot_general(qh, k2, nt, preferred_element_type=F32) * ATTN_SCALE - (slopes[h] * dilation) * rel
                p = jnp.exp(jnp.where(valid, s, NEG_INF) - lh)
                dob = do_ref[rows, cols].astype(_CD)
                ds = p * (lax.dot_general(dob, v2, nt, preferred_element_type=F32) + shift)
                dsb = (ds * ATTN_SCALE).astype(_CD)
                dq_ref[rows, cols] = jnp.dot(dsb, k2, preferred_element_type=F32)
                dk_s[cols, window] += lax.dot_general(qh, dsb, tn, preferred_element_type=F32)
                dv_s[cols, window] += lax.dot_general(dob, p.astype(_CD), tn, preferred_element_type=F32)
        for i in range(4):
            rows = slice(i * DIL_W, (i + 1) * DIL_W)
            done = slice((i + 1) * DIL_W, (i + 2) * DIL_W)
            nat = _dil_rows(4 * n + i, S, dilation)
            dkb, dvb = dk_s[:, done].T, dv_s[:, done].T
            for half in range(DIL_OUT // LANES):
                cols = slice(half * LANES, (half + 1) * LANES)
                dqn_ref[half, nat, :] = dq_ref[rows, cols]
                dkn_ref[half, nat, :] = dkb[:, cols]
                dvn_ref[half, nat, :] = dvb[:, cols]
        dk_s[:, CHUNK:] = dk_s[:, :DIL_W]
        dv_s[:, CHUNK:] = dv_s[:, :DIL_W]

    def cur(which):
        return pl.BlockSpec((None, CHUNK, DIL_OUT), lambda s: (which, nchunk - 1 - s, 0))

    def prev(which):
        return pl.BlockSpec((None, DIL_W, DIL_OUT), lambda s: (which, jnp.maximum(4 * (nchunk - 1 - s) - 1, 0), 0))

    whole = pl.BlockSpec(SPLIT(S), lambda s: (0, 0, 0))
    shp = jax.ShapeDtypeStruct(SPLIT(S), F32)
    tile = pltpu.VMEM((CHUNK, DIL_OUT), F32)
    return _pcall(body, name=name, grid=(nchunk,),
                  in_specs=[cur(0), cur(1), cur(2), prev(1), prev(2), whole, whole],
                  out_specs=[whole, whole, whole], out_shape=[shp, shp, shp],
                  scratch_shapes=[pltpu.VMEM((DIL_OUT, CHUNK + DIL_W), F32), pltpu.VMEM((DIL_OUT, CHUNK + DIL_W), F32),
                                  tile, tile, tile],
                  compiler_params=pltpu.CompilerParams(dimension_semantics=("arbitrary",),
                                                       vmem_limit_bytes=VMEM_LIMIT_RESIDENT))(
        qkv, qkv, qkv, qkv, qkv, stats, do)


def _dil_mix_fwd(os_, ls_, *, name, tm=512):
    nh, S, _ = os_[0].shape

    def body(o0, o1, o2, l0, l1, l2, out_ref):
        for half in range(nh):
            ls = [l0[half], l1[half], l2[half]]
            m = jnp.maximum(jnp.maximum(ls[0], ls[1]), ls[2])
            es = [jnp.exp(l - m) for l in ls]
            den = es[0] + es[1] + es[2]
            mixed = (es[0] * o0[half] + es[1] * o1[half] + es[2] * o2[half]) / den
            out_ref[:, half * LANES:(half + 1) * LANES] = mixed.astype(out_ref.dtype)

    halves = pl.BlockSpec((nh, tm, LANES), lambda i: (0, i, 0))
    row = pl.BlockSpec((tm, nh * LANES), lambda i: (i, 0))
    return _pcall(body, name=name, grid=(S // tm,), in_specs=[halves] * 6, out_specs=row,
                  out_shape=jax.ShapeDtypeStruct((S, nh * LANES), _CD), compiler_params=_params("parallel"))(*os_, *ls_)


def _dil_mix_bwd(doa, os_, ls_, *, name, tm=512, after=None):
    nh, S, _ = os_[0].shape

    def body(d_ref, o0, o1, o2, l0, l1, l2, do0, do1, do2, st0, st1, st2):
        first = lax.broadcasted_iota(jnp.int32, (tm, LANES), 1) % HEAD_DIM < STAT_OFFSET
        for half in range(nh):
            dv = d_ref[:, half * LANES:(half + 1) * LANES]
            ls = [l0[half], l1[half], l2[half]]
            m = jnp.maximum(jnp.maximum(ls[0], ls[1]), ls[2])
            es = [jnp.exp(l - m) for l in ls]
            den = es[0] + es[1] + es[2]
            al = [e / den for e in es]
            da = [_head_sum(dv * o[half]) for o in (o0, o1, o2)]
            mean = al[0] * da[0] + al[1] * da[1] + al[2] * da[2]
            for a, l, do_ref, st_ref in zip(al, ls, (do0, do1, do2), (st0, st1, st2)):
                do_ref[half] = a * dv
                st_ref[half] = jnp.where(first, l, -a * mean)

    halves = pl.BlockSpec((nh, tm, LANES), lambda i: (0, i, 0))
    row = pl.BlockSpec((tm, nh * LANES), lambda i: (i, 0))
    shp = jax.ShapeDtypeStruct((nh, S, LANES), F32)
    return _pcall(body, after, name=name, grid=(S // tm,), in_specs=[row] + [halves] * 6, out_specs=[halves] * 6,
                  out_shape=[shp] * 6, compiler_params=_params("parallel"))(doa, *os_, *ls_)


FOX_T = 512


PACK = 2 * HEAD_DIM
HEAD_PAIRS = N_FOX_HEADS // 2
FOX_HPS = 8
Q_BLOCK0 = 0
K_BLOCK0 = FOX_WIDTH // PACK
V_BLOCK0 = 2 * FOX_WIDTH // PACK


def _pieces(x):
    hi = x.astype(jnp.bfloat16).astype(F32)
    r = x - hi
    mid = r.astype(jnp.bfloat16).astype(F32)
    lo = (r - mid).astype(jnp.bfloat16).astype(F32)
    return [hi, mid, lo]


def _extras(first, second, rows):
    lane = lax.broadcasted_iota(jnp.int32, (rows, HEAD_DIM), 1)
    out = jnp.zeros((rows, HEAD_DIM), F32)
    for base, triple in ((0, first), (3, second)):
        if all(isinstance(v, float) for v in triple) and len(set(triple)) == 1:
            if triple[0] != 0.0:
                out = jnp.where(jnp.logical_and(lane >= base, lane < base + 3), triple[0], out)
        else:
            for idx, val in enumerate(triple):
                out = jnp.where(lane == base + idx, val, out)
    return out


def _head_column(c, h):
    lane = lax.broadcasted_iota(jnp.int32, c.shape, 1)
    return jnp.sum(jnp.where(lane == h, c, 0.0), axis=1, keepdims=True)


ONES3 = [1.0, 1.0, 1.0]
ZEROS3 = [0.0, 0.0, 0.0]


def _fox_pack_fwd(qkv, c, *, name, tm=1024):
    S = qkv.shape[0]

    def body(q_ref, k_ref, v_ref, c_ref, qo_ref, ko_ref, vo_ref):
        hp = pl.program_id(1)
        cv = c_ref[...]
        v_extras = jnp.where(lax.broadcasted_iota(jnp.int32, (tm, HEAD_DIM), 1) < 3, 1.0, 0.0).astype(vo_ref.dtype)
        for hh in range(2):
            ch = _pieces(_head_column(cv, 2 * hp + hh))
            src = slice(hh * HEAD_DIM, (hh + 1) * HEAD_DIM)
            lo = slice(hh * PACK, hh * PACK + HEAD_DIM)
            hi = slice(hh * PACK + HEAD_DIM, (hh + 1) * PACK)
            qo_ref[:, lo] = (q_ref[:, src].astype(F32) * ATTN_SCALE).astype(qo_ref.dtype)
            qo_ref[:, hi] = _extras(ch, ONES3, tm).astype(qo_ref.dtype)
            ko_ref[:, lo] = k_ref[:, src]
            ko_ref[:, hi] = _extras(ONES3, [-p for p in ch], tm).astype(ko_ref.dtype)
            vo_ref[:, lo] = v_ref[:, src]
            vo_ref[:, hi] = v_extras

    def src(block0):
        return pl.BlockSpec((tm, PACK), lambda i, hp: (i, block0 + hp))

    out = pl.BlockSpec((tm, 2 * PACK), lambda i, hp: (i, hp))
    shp = jax.ShapeDtypeStruct((S, N_FOX_HEADS * PACK), _CD)
    return _pcall(body, name=name, grid=(S // tm, HEAD_PAIRS),
                  in_specs=[src(Q_BLOCK0), src(K_BLOCK0), src(V_BLOCK0), pl.BlockSpec((tm, PACK), lambda i, hp: (i, 0))],
                  out_specs=[out, out, out], out_shape=[shp, shp, shp],
                  compiler_params=_params("parallel", "parallel"))(qkv, qkv, qkv, c)


def _fox_fwd(qp, kp, vp, *, name):
    S = qp.shape[0]
    nt = S // FOX_T
    nt_dims = (((1,), (1,)), ((), ()))
    tn_dims = (((0,), (0,)), ((), ()))

    def body(i_tab, j_tab, q_ref, k_ref, v_ref, o_ref, l_ref, m_s, acc_s):
        t = pl.program_id(1)
        i, j = i_tab[t], j_tab[t]

        @pl.when(j == 0)
        def _():
            m_s[...] = jnp.full((FOX_HPS, 1, FOX_T), NEG_INF, F32)
            acc_s[...] = jnp.zeros((FOX_HPS, PACK, FOX_T), F32)

        def tile(diagonal):
            for hh in range(FOX_HPS):
                cols = slice(hh * PACK, (hh + 1) * PACK)
                st = lax.dot_general(k_ref[:, cols], q_ref[:, cols], nt_dims, preferred_element_type=F32)
                if diagonal:
                    key = lax.broadcasted_iota(jnp.int32, (FOX_T, FOX_T), 0)
                    qry = lax.broadcasted_iota(jnp.int32, (FOX_T, FOX_T), 1)
                    st = jnp.where(key <= qry, st, NEG_INF)
                m_old = m_s[hh]
                m_new = jnp.maximum(m_old, jnp.max(st, axis=0, keepdims=True))
                pt = jnp.exp(st - m_new)
                acc_s[hh] = jnp.exp(m_old - m_new) * acc_s[hh] + lax.dot_general(
                    v_ref[:, cols], pt.astype(_CD), tn_dims, preferred_element_type=F32)
                m_s[hh] = m_new

        @pl.when(j < i)
        def _():
            tile(False)

        @pl.when(j == i)
        def _():
            tile(True)
            for hh in range(FOX_HPS):
                acc = acc_s[hh]
                den = acc[HEAD_DIM:HEAD_DIM + 1, :]
                cols = slice(hh * HEAD_DIM, (hh + 1) * HEAD_DIM)
                o_ref[:, cols] = (acc[:HEAD_DIM, :] / den).T
                l_ref[:, cols] = jnp.broadcast_to(m_s[hh] + jnp.log(den), (HEAD_DIM, FOX_T)).T

    pairs = [(i, j) for i in range(nt) for j in range(i + 1)]
    i_tab = jnp.asarray([p[0] for p in pairs], jnp.int32)
    j_tab = jnp.asarray([p[1] for p in pairs], jnp.int32)
    qs = pl.BlockSpec((FOX_T, FOX_HPS * PACK), lambda hp, t, it, jt: (it[t], hp))
    ks = pl.BlockSpec((FOX_T, FOX_HPS * PACK), lambda hp, t, it, jt: (jt[t], hp))
    os_ = pl.BlockSpec((FOX_T, FOX_HPS * HEAD_DIM), lambda hp, t, it, jt: (it[t], hp))
    shp = jax.ShapeDtypeStruct((S, FOX_WIDTH), F32)
    grid_spec = pltpu.PrefetchScalarGridSpec(
        num_scalar_prefetch=2, grid=(N_FOX_HEADS // FOX_HPS, len(pairs)), in_specs=[qs, ks, ks], out_specs=[os_, os_],
        scratch_shapes=[pltpu.VMEM((FOX_HPS, 1, FOX_T), F32), pltpu.VMEM((FOX_HPS, PACK, FOX_T), F32)])
    return _pcall(body, name=name, grid_spec=grid_spec, out_shape=[shp, shp],
                  compiler_params=_params("parallel", "arbitrary"))(i_tab, j_tab, qp, kp, vp)


def _fox_pack_bwd(qkv, c, o, lse, do, *, name, tm=1024, after=None):
    S = qkv.shape[0]

    def body(q_ref, c_ref, o_ref, l_ref, do_ref, qo_ref, do_out_ref):
        hp = pl.program_id(1)
        cv = c_ref[...]
        for hh in range(2):
            src = slice(hh * HEAD_DIM, (hh + 1) * HEAD_DIM)
            lo = slice(hh * PACK, hh * PACK + HEAD_DIM)
            hi = slice(hh * PACK + HEAD_DIM, (hh + 1) * PACK)
            shift = _head_column(cv, 2 * hp + hh) - l_ref[:, hh * HEAD_DIM:hh * HEAD_DIM + 1]
            dov = do_ref[:, src]
            dsum = jnp.sum(dov * o_ref[:, src], axis=-1, keepdims=True)
            qo_ref[:, lo] = (q_ref[:, src].astype(F32) * ATTN_SCALE).astype(qo_ref.dtype)
            qo_ref[:, hi] = _extras(_pieces(shift), ONES3, tm).astype(qo_ref.dtype)
            do_out_ref[:, lo] = dov.astype(do_out_ref.dtype)
            do_out_ref[:, hi] = _extras(_pieces(-dsum), ZEROS3, tm).astype(do_out_ref.dtype)

    pair = pl.BlockSpec((tm, PACK), lambda i, hp: (i, hp))
    out = pl.BlockSpec((tm, 2 * PACK), lambda i, hp: (i, hp))
    shp = jax.ShapeDtypeStruct((S, N_FOX_HEADS * PACK), _CD)
    return _pcall(body, after, name=name, grid=(S // tm, HEAD_PAIRS),
                  in_specs=[pl.BlockSpec((tm, PACK), lambda i, hp: (i, Q_BLOCK0 + hp)),
                            pl.BlockSpec((tm, PACK), lambda i, hp: (i, 0)), pair, pair, pair],
                  out_specs=[out, out], out_shape=[shp, shp],
                  compiler_params=_params("parallel", "parallel"))(qkv, c, o, lse, do)


def _fox_bwd(qp, kp, vp, dop, *, name):
    S = qp.shape[0]
    nt = S // FOX_T
    nt_dims = (((1,), (1,)), ((), ()))
    tn_dims = (((0,), (0,)), ((), ()))

    def body(i_tab, j_tab, q_ref, k_ref, v_ref, do_ref, dq_ref, dk_ref, dv_ref, dc_ref, dr_ref,
             dq_s, dk_s, dv_s, dc_s, dr_s):
        t = pl.program_id(1)
        i, j = i_tab[t], j_tab[t]

        @pl.when(t == 0)
        def _():
            dq_s[...] = jnp.zeros((S, FOX_HPS * PACK), F32)
            dr_s[...] = jnp.zeros((FOX_HPS, 1, S), F32)

        @pl.when(i == j)
        def _():
            dk_s[...] = jnp.zeros((FOX_T, FOX_HPS * PACK), F32)
            dv_s[...] = jnp.zeros((FOX_T, FOX_HPS * PACK), F32)
            dc_s[...] = jnp.zeros((FOX_HPS, FOX_T, 1), F32)

        def tile(diagonal):
            rows = pl.ds(pl.multiple_of(i * FOX_T, FOX_T), FOX_T)
            for hh in range(FOX_HPS):
                cols = slice(hh * PACK, (hh + 1) * PACK)
                qv, kv, vv, dov = q_ref[:, cols], k_ref[:, cols], v_ref[:, cols], do_ref[:, cols]
                pt = jnp.exp(lax.dot_general(kv, qv, nt_dims, preferred_element_type=F32))
                if diagonal:
                    key = lax.broadcasted_iota(jnp.int32, (FOX_T, FOX_T), 0)
                    qry = lax.broadcasted_iota(jnp.int32, (FOX_T, FOX_T), 1)
                    pt = jnp.where(key <= qry, pt, 0.0)
                dst = pt * lax.dot_general(vv, dov, nt_dims, preferred_element_type=F32)
                dsb = dst.astype(_CD)
                dc_s[hh] += jnp.sum(dst, axis=1, keepdims=True)
                dr_s[hh, :, rows] += jnp.sum(dst, axis=0, keepdims=True)
                dv_s[:, cols] += jnp.dot(pt.astype(_CD), dov, preferred_element_type=F32)
                dk_s[:, cols] += jnp.dot(dsb, qv, preferred_element_type=F32)
                dq_s[rows, cols] += lax.dot_general(dsb, kv, tn_dims, preferred_element_type=F32)

        @pl.when(i > j)
        def _():
            tile(False)

        @pl.when(i == j)
        def _():
            tile(True)

        @pl.when(i == nt - 1)
        def _():
            for hh in range(FOX_HPS):
                src = slice(hh * PACK, hh * PACK + HEAD_DIM)
                dst_cols = slice(hh * HEAD_DIM, (hh + 1) * HEAD_DIM)
                dk_ref[:, dst_cols] = dk_s[:, src].astype(dk_ref.dtype)
                dv_ref[:, dst_cols] = dv_s[:, src].astype(dv_ref.dtype)
                dc_ref[:, dst_cols] = jnp.broadcast_to(dc_s[hh], (FOX_T, HEAD_DIM))

        @pl.when(t == len(pairs) - 1)
        def _():
            for hh in range(FOX_HPS):
                dq_ref[:, hh * HEAD_DIM:(hh + 1) * HEAD_DIM] = (
                    dq_s[:, hh * PACK:hh * PACK + HEAD_DIM] * ATTN_SCALE).astype(dq_ref.dtype)
            dr_ref[...] = dr_s[...]

    pairs = [(i, j) for j in range(nt) for i in range(j, nt)]
    i_tab = jnp.asarray([p[0] for p in pairs], jnp.int32)
    j_tab = jnp.asarray([p[1] for p in pairs], jnp.int32)
    wide, narrow = FOX_HPS * PACK, FOX_HPS * HEAD_DIM
    qs = pl.BlockSpec((FOX_T, wide), lambda hp, t, it, jt: (it[t], hp))
    ks = pl.BlockSpec((FOX_T, wide), lambda hp, t, it, jt: (jt[t], hp))
    whole = pl.BlockSpec((S, narrow), lambda hp, t, it, jt: (0, hp))
    cs = pl.BlockSpec((FOX_T, narrow), lambda hp, t, it, jt: (jt[t], hp))
    rs = pl.BlockSpec((FOX_HPS, 1, S), lambda hp, t, it, jt: (hp, 0, 0))
    shp = jax.ShapeDtypeStruct((S, FOX_WIDTH), _CD)
    grid_spec = pltpu.PrefetchScalarGridSpec(
        num_scalar_prefetch=2, grid=(N_FOX_HEADS // FOX_HPS, len(pairs)), in_specs=[qs, ks, ks, qs],
        out_specs=[whole, cs, cs, cs, rs],
        scratch_shapes=[pltpu.VMEM((S, wide), F32), pltpu.VMEM((FOX_T, wide), F32),
                        pltpu.VMEM((FOX_T, wide), F32), pltpu.VMEM((FOX_HPS, FOX_T, 1), F32),
                        pltpu.VMEM((FOX_HPS, 1, S), F32)])
    return _pcall(body, name=name, grid_spec=grid_spec,
                  out_shape=[shp, shp, shp, jax.ShapeDtypeStruct((S, FOX_WIDTH), F32),
                             jax.ShapeDtypeStruct((N_FOX_HEADS, 1, S), F32)],
                  compiler_params=_params("parallel", "arbitrary"))(i_tab, j_tab, qp, kp, vp, dop)


def _layer_step(x, tgt, w, p, late_weights=None, grad_sink=None, after=None, first_weights=None):
    S = x.shape[0]
    after_norm, after_proj = after if after is not None else (None, None)
    h = _rms_fwd(x, p["norm_mix_g"], name="rms_mix", after=after_norm)
    if first_weights is not None:
        w = {**w, **first_weights(h)}
    qkv = _mm(h, w["qkv"][:, 3 * DIL_WIDTH:], name="proj_fox", out_dtype=_CD, tn=768, tm=2048, after=after_proj)
    dil_qkv = _proj_dil(h, w["qkv"], name="proj_dil")
    zf = _mm(h, w["f"], name="proj_f")
    gl = _mm(h, w["g"], name="proj_gate", tn=1024, out_dtype=_CD)

    dil_o, dil_l = [], []
    for g in range(N_DIL_GROUPS):
        og, lg = _dil_fwd(dil_qkv[g], g, name=f"dil_fwd{g}")
        dil_o.append(og), dil_l.append(lg)
    o_a = _dil_mix_fwd(dil_o, dil_l, name="dil_mix")

    c = _fox_cumsum(zf, p["b_fgt"], name="fox_cumsum")
    fqp, fkp, fvp = _fox_pack_fwd(qkv, c, name="fox_pack")
    o_b, flse = _fox_fwd(fqp, fkp, fvp, name="fox_fwd")

    if late_weights is not None:
        w = {**w, **late_weights(o_b)}
    y_a = _mm(o_a, w["dil_out"], name="y_a", tn=1024, out_dtype=_CD)
    y_b = _mm(o_b, w["fox_out"], name="y_b", tn=1024, out_dtype=_CD)
    merged, x1, h2 = _gated_mix_out(gl, p["b_gate"], y_a, y_b, w["out"], x, p["norm_ffn_g"], name="mix_out")
    gate, up, act = _ffn_in_act(h2, w["ffn_in"], name="ffn_in")
    loss, dx2, dg_final = _ffn_down_loss(act, w["ffn_down"], x1, p["norm_final_g"], tgt, name="ffn_down_loss")

    gw_ffn_down = _mm(act, dx2, name="gw_ffn_down", ta=True, out_dtype=_CD, tm=1408)
    dgu = _d_swiglu(dx2, w["ffn_down"], gate, up, name="d_swiglu")
    gw_ffn_in = _mm(h2, dgu, name="gw_ffn_in", ta=True, out_dtype=_CD, tn=1408, out_blocks=1408, b_halves=True)
    sink = grad_sink if grad_sink is not None else (lambda group, grads: None)
    tok = sink("ffn", dict(ffn_in=gw_ffn_in, ffn_down=gw_ffn_down))
    dx1, dg_ffn = _mm(dgu, w["ffn_in"], name="d_h2", tb=True, tk=1408, b_blocks=True, tm=1024, a_halves=True,
                      rms_bwd=(x1, p["norm_ffn_g"], dx2), after=tok)

    gw_out = _mm(merged, dx1, name="gw_out", ta=True, out_dtype=_CD)
    dy_a, dy_b, dgl, db_gate = _gate_bwd(dx1, w["out"], gl, p["b_gate"], y_a, y_b, name="gate_bwd")
    do_a = _mm(dy_a, w["dil_out"], name="d_o_a", tb=True)
    gw_dil_out = _mm(o_a, dy_a, name="gw_dil_out", ta=True, out_dtype=_CD, tn=1024)
    do_b = _mm(dy_b, w["fox_out"], name="d_o_b", tb=True)
    gw_fox_out = _mm(o_b, dy_b, name="gw_fox_out", ta=True, out_dtype=_CD, tn=1024)
    tok = sink("mix", dict(dil_out=gw_dil_out, fox_out=gw_fox_out, out=gw_out))

    bqp, bdop = _fox_pack_bwd(qkv, c, o_b, flse, do_b, name="fox_pack_bwd", after=tok)
    dqp, dkp, dvp, dck, dcq = _fox_bwd(bqp, fkp, fvp, bdop, name="fox_bwd")
    dc = dcq[:, 0, :].T - dck.reshape(S, N_FOX_HEADS, HEAD_DIM)[:, :, 0]
    dc = jnp.pad(dc, ((0, 0), (0, F_PAD - N_FOX_HEADS)))
    dzf, db_fgt = _fox_cumsum_bwd(dc, zf, p["b_fgt"], name="fox_cumsum_bwd")

    douts = _dil_mix_bwd(do_a, dil_o, dil_l, name="dil_mix_bwd", after=tok)
    dqs, dks, dvs = [], [], []
    for g in range(N_DIL_GROUPS):
        dq, dk, dv = _dil_bwd(dil_qkv[g], douts[3 + g], douts[g], g, name=f"dil_bwd{g}")
        for parts, t in ((dqs, dq), (dks, dk), (dvs, dv)):
            parts.extend([t[0].astype(_CD), t[1].astype(_CD)])
    dqkv = jnp.concatenate(dqs + dks + dvs + [dqp, dkp, dvp], axis=1)

    gw_qkv = _mm(h, dqkv, name="gw_qkv", ta=True, out_dtype=_CD, tn=768)
    gw_g = _mm(h, dgl, name="gw_gate", ta=True, out_dtype=_CD)
    gw_f = _mm(h, dzf, name="gw_f", ta=True, out_dtype=_CD)
    tok = sink("in", dict(qkv=gw_qkv, f=gw_f, g=gw_g))
    dx, dg_mix = _mm(dqkv, w["qkv"], name="d_h", tb=True, tk=QKV_COLS, tm=256, more=((dgl, w["g"]), (dzf, w["f"])),
                     rms_bwd=(x, p["norm_mix_g"], dx1), after=tok)

    gw = dict(qkv=gw_qkv, f=gw_f, g=gw_g, dil_out=gw_dil_out, fox_out=gw_fox_out, out=gw_out, ffn_in=gw_ffn_in,
              ffn_down=gw_ffn_down)
    small = dict(norm_mix_g=dg_mix, b_fgt=db_fgt, b_gate=db_gate, norm_ffn_g=dg_ffn, norm_final_g=dg_final)
    return loss, dx, gw, small


def _position():
    return lax.axis_index("x"), lax.axis_index("y"), lax.axis_index("c")


def _other_chips(x, y):
    return [(1 - x, y), (x, 1 - y), (1 - x, 1 - y)]


ROW_TILE = 16


def _row_chunks(rows, want=4):
    n = want
    while n > 1 and rows % (n * ROW_TILE):
        n //= 2
    return n


SEM_SPEC = pl.BlockSpec(memory_space=pltpu.SEMAPHORE)
ANY_SPEC = pl.BlockSpec(memory_space=pl.ANY)
DATAFLOW = pltpu.SideEffectType.DATAFLOW_SIDE_EFFECTING


def _in_hbm(a):
    return pltpu.with_memory_space_constraint(a, pltpu.HBM)


def _split_copy_start(srcs, land_shapes, copies, after, *, name):
    n, m = len(srcs), len(land_shapes)

    def body(*refs):
        src_refs, land_refs = refs[:n], refs[n:n + m]
        send_sems, recv_sems = refs[n + m + 1], refs[n + m + 2]
        token = refs[-1]
        x, y, c = _position()
        for k, (src, dst, peer) in enumerate(copies(x, y, c, src_refs, land_refs)):
            pltpu.make_async_remote_copy(src_ref=src, dst_ref=dst, send_sem=send_sems.at[k], recv_sem=recv_sems.at[k],
                                         device_id=peer, device_id_type=MESH).start()
        token[...] = jnp.zeros_like(token)

    lands = [lax.empty(s.shape, s.dtype) for s in land_shapes]
    count = len(copies(0, 0, 0, srcs, lands))
    out = _pcall(
        body, name=name,
        out_shape=(pltpu.SemaphoreType.DMA((count,)), pltpu.SemaphoreType.DMA((count,)),
                   *[pltpu.HBM(s.shape, s.dtype) for s in srcs], *[pltpu.HBM(s.shape, s.dtype) for s in land_shapes],
                   jax.ShapeDtypeStruct((8, 128), F32)),
        in_specs=[HBM_SPEC] * (n + m) + [ANY_SPEC],
        out_specs=(SEM_SPEC, SEM_SPEC, *[HBM_SPEC] * (n + m), pl.BlockSpec(memory_space=pltpu.VMEM)),
        input_output_aliases={k: 2 + k for k in range(n + m)},
        compiler_params=pltpu.CompilerParams(has_side_effects=DATAFLOW),
    )(*[_in_hbm(s) for s in srcs], *[_in_hbm(l) for l in lands], after)
    return out[0], out[1], list(out[2:2 + n]), list(out[2 + n:2 + n + m]), out[-1]


def _split_copy_wait(send_sems, recv_sems, srcs, lands, copies, after, *, name):
    n, m = len(srcs), len(lands)

    def body(*refs):
        src_refs, land_refs = refs[:n], refs[n:n + m]
        send, recv = refs[n + m], refs[n + m + 1]
        x, y, c = _position()
        for k, (src, dst, peer) in enumerate(copies(x, y, c, src_refs, land_refs)):
            cp = pltpu.make_async_remote_copy(src_ref=src, dst_ref=dst, send_sem=send.at[k], recv_sem=recv.at[k],
                                              device_id=peer, device_id_type=MESH)
            cp.wait_send()
            cp.wait_recv()

    afters = list(after) if isinstance(after, (list, tuple)) else [after]
    out = _pcall(
        body, name=name,
        out_shape=tuple(pltpu.HBM(s.shape, s.dtype) for s in list(srcs) + list(lands)),
        in_specs=[HBM_SPEC] * (n + m) + [SEM_SPEC, SEM_SPEC] + [ANY_SPEC] * len(afters),
        out_specs=tuple([HBM_SPEC] * (n + m)),
        input_output_aliases={k: k for k in range(n + m)},
        compiler_params=pltpu.CompilerParams(has_side_effects=DATAFLOW),
    )(*srcs, *lands, send_sems, recv_sems, *afters)
    return list(out[:n]), list(out[n:])


def _gather_copies(x, y, c, shard_refs, land_refs):
    out = []
    for s, l in zip(shard_refs, land_refs):
        half = s.shape[0] // 2
        nq = _row_chunks(half)
        for cx, cy in _other_chips(x, y):
            for q in range(nq):
                rows = pl.ds(c * half + q * (half // nq), half // nq)
                out.append((s.at[rows, :], l.at[2 * x + y, rows, :], (cx, cy, c)))
    return out


def _gather_whole_copies(x, y, c, shard_refs, land_refs):
    out = []
    for s, l in zip(shard_refs, land_refs):
        nq = _row_chunks(s.shape[0])
        for cx, cy in _other_chips(x, y):
            for q in range(nq):
                rows = pl.ds(q * (s.shape[0] // nq), s.shape[0] // nq)
                out.append((s.at[rows, :], l.at[2 * x + y, rows, :], (cx, cy, c)))
    return out


def _scatter_all_copies(x, y, c, block_refs, land_refs):
    out = []
    for g, l in zip(block_refs, land_refs):
        half = g.shape[1] // 2
        nq = _row_chunks(half)
        size = half // nq
        for q in range(nq):
            rows = pl.ds((1 - c) * half + q * size, size)
            out.append((g.at[2 * x + y, rows, :], l.at[0, pl.ds(q * size, size), :], (x, y, 1 - c)))
        for r, (cx, cy) in enumerate(_other_chips(x, y)):
            for j in range(2):
                h = c if j == 0 else 1 - c
                for q in range(nq):
                    rows = pl.ds(h * half + q * size, size)
                    out.append((g.at[2 * cx + cy, rows, :], l.at[1 + 2 * r + j, pl.ds(q * size, size), :], (cx, cy, h)))
    return out


def _forward_halves(lands, *, name):
    n = len(lands)

    def body(*refs):
        ins = refs[:n]
        send_sems, recv_sems = refs[2 * n:]
        x, y, c = _position()
        copies = []
        for w in range(n):
            half = ins[w].shape[1] // 2
            for r, (cx, cy) in enumerate(_other_chips(x, y)):
                blk = ins[w].at[2 * cx + cy, pl.ds(c * half, half), :]
                cp = pltpu.make_async_remote_copy(src_ref=blk, dst_ref=blk, send_sem=send_sems.at[w, r],
                                                  recv_sem=recv_sems.at[w, r], device_id=(x, y, 1 - c),
                                                  device_id_type=MESH)
                cp.start()
                copies.append(cp)
        for w in range(n):
            half = ins[w].shape[1] // 2
            for r, (cx, cy) in enumerate(_other_chips(x, y)):
                blk = ins[w].at[2 * cx + cy, pl.ds((1 - c) * half, half), :]
                pltpu.make_async_remote_copy(src_ref=blk, dst_ref=blk, send_sem=send_sems.at[w, r],
                                             recv_sem=recv_sems.at[w, r], device_id=(x, y, 1 - c),
                                             device_id_type=MESH).wait_recv()
        for cp in copies:
            cp.wait_send()

    return _pcall(
        body, name=name, in_specs=[HBM_SPEC] * n, out_specs=[HBM_SPEC] * n,
        out_shape=[jax.ShapeDtypeStruct(l.shape, l.dtype) for l in lands],
        input_output_aliases={k: k for k in range(n)},
        scratch_shapes=[pltpu.SemaphoreType.DMA((n, 3)), pltpu.SemaphoreType.DMA((n, 3))],
    )(*lands)


def _share_halves(halves):
    n = len(halves)

    def body(*refs):
        ins, outs = refs[:n], refs[n:2 * n]
        send_sems, recv_sems = refs[2 * n:]
        x, y, c = _position()
        copies = []
        for w in range(n):
            cp = pltpu.make_async_remote_copy(src_ref=ins[w], dst_ref=outs[w], send_sem=send_sems.at[w],
                                              recv_sem=recv_sems.at[w], device_id=(x, y, 1 - c), device_id_type=MESH)
            cp.start()
            copies.append(cp)
        for cp in copies:
            cp.wait()

    return _pcall(
        body, name="share_halves", in_specs=[HBM_SPEC] * n, out_specs=[HBM_SPEC] * n,
        out_shape=[jax.ShapeDtypeStruct(h.shape, h.dtype) for h in halves],
        scratch_shapes=[pltpu.SemaphoreType.DMA((n,)), pltpu.SemaphoreType.DMA((n,))],
    )(*halves)


def _sum_small(part):
    rows, width = part.shape

    def body(x_ref, out_ref, all_ref, send_sems, recv_sems):
        x, y, c = _position()
        me, sibling = (x, y, c), (x, y, 1 - c)
        chips = _other_chips(x, y)

        def block(px, py, pc):
            return all_ref.at[pl.ds((4 * px + 2 * py + pc) * rows, rows), :]

        def copy(k, blk, to, src=None):
            return pltpu.make_async_remote_copy(
                src_ref=block(*blk) if src is None else src, dst_ref=block(*blk), send_sem=send_sems.at[k],
                recv_sem=recv_sems.at[k], device_id=to, device_id_type=MESH)

        all_ref[pl.ds((4 * x + 2 * y + c) * rows, rows), :] = x_ref[...]
        first = [copy(0, me, sibling, src=x_ref)]
        first += [copy(1 + j, me, (*chip, c), src=x_ref) for j, chip in enumerate(chips)]
        for cp in first:
            cp.start()
        passed = [copy(4 + j, (*chip, c), sibling) for j, chip in enumerate(chips)]
        for j, chip in enumerate(chips):
            copy(1 + j, (*chip, c), me).wait_recv()
            passed[j].start()
        copy(0, sibling, me).wait_recv()
        for j, chip in enumerate(chips):
            copy(4 + j, (*chip, 1 - c), me).wait_recv()
        for cp in first + passed:
            cp.wait_send()
        total = all_ref[0:rows, :]
        for d in range(1, 8):
            total = total + all_ref[d * rows:(d + 1) * rows, :]
        out_ref[...] = total

    vm = pl.BlockSpec(memory_space=pltpu.VMEM)
    return _pcall(
        body, name="sum_small", in_specs=[vm], out_specs=vm, out_shape=jax.ShapeDtypeStruct((rows, width), F32),
        scratch_shapes=[pltpu.VMEM((8 * rows, width), F32), pltpu.SemaphoreType.DMA((7,)), pltpu.SemaphoreType.DMA((7,))],
    )(part)


def _row_tile(R, C, itemsize=4, budget=1 << 20):
    for t in (512, 256, 128, 64, 32, 16, 8):
        if R % t == 0 and t * C * itemsize <= budget:
            return t
    return R


def _add_all(g, recv, where, *, name):
    _, R, C = g.shape
    half = R // 2
    t = _row_tile(half, C)
    nb = half // t

    def body(w_ref, g_ref, r_ref, o_ref):
        total = g_ref[0].astype(F32)
        for k in range(7):
            total = total + r_ref[k].astype(F32)
        o_ref[...] = total

    grid_spec = pltpu.PrefetchScalarGridSpec(
        num_scalar_prefetch=1, grid=(nb,),
        in_specs=[pl.BlockSpec((1, t, C), lambda i, wr: (wr[0], wr[1] * nb + i, 0)),
                  pl.BlockSpec((7, t, C), lambda i, wr: (0, i, 0))],
        out_specs=pl.BlockSpec((t, C), lambda i, wr: (i, 0)))
    return _pcall(body, name=name, grid_spec=grid_spec, out_shape=jax.ShapeDtypeStruct((half, C), F32),
                  compiler_params=_params("parallel"))(where, g, recv)


def _adamw(w, g, m, v, *, name):
    R, C = w.shape
    t = _row_tile(R, C)
    c1 = 1.0 - ADAM_B1 ** ADAM_STEP
    c2 = 1.0 - ADAM_B2 ** ADAM_STEP

    def body(w_ref, g_ref, m_ref, v_ref, d_ref, nm_ref, nv_ref):
        gv = g_ref[...]
        mn = ADAM_B1 * m_ref[...] + (1.0 - ADAM_B1) * gv
        vn = ADAM_B2 * v_ref[...] + (1.0 - ADAM_B2) * (gv * gv)
        d_ref[...] = -ADAM_LR * ((mn / c1) / (jnp.sqrt(vn / c2) + ADAM_EPS) + ADAM_WD * w_ref[...])
        nm_ref[...] = mn
        nv_ref[...] = vn

    blk = pl.BlockSpec((t, C), lambda i: (i, 0))
    shp = jax.ShapeDtypeStruct((R, C), F32)
    return _pcall(body, name=name, grid=(R // t,), in_specs=[blk] * 4, out_specs=[blk] * 3, out_shape=[shp] * 3,
                  compiler_params=_params("parallel"))(w, g, m, v)


def _adamw_halves(w, mine, theirs, m, v, core, *, name):
    R, C = w.shape
    half = R // 2
    t = _row_tile(half, C)
    nbh = half // t
    c1 = 1.0 - ADAM_B1 ** ADAM_STEP
    c2 = 1.0 - ADAM_B2 ** ADAM_STEP

    def body(core_ref, w_ref, a_ref, b_ref, m_ref, v_ref, g_ref, d_ref, nm_ref, nv_ref):
        gv = jnp.where(pl.program_id(0) // nbh == core_ref[0], a_ref[...], b_ref[...])
        mn = ADAM_B1 * m_ref[...] + (1.0 - ADAM_B1) * gv
        vn = ADAM_B2 * v_ref[...] + (1.0 - ADAM_B2) * (gv * gv)
        g_ref[...] = gv
        d_ref[...] = -ADAM_LR * ((mn / c1) / (jnp.sqrt(vn / c2) + ADAM_EPS) + ADAM_WD * w_ref[...])
        nm_ref[...] = mn
        nv_ref[...] = vn

    blk = pl.BlockSpec((t, C), lambda i, cr: (i, 0))
    hblk = pl.BlockSpec((t, C), lambda i, cr: (i % nbh, 0))
    shp = jax.ShapeDtypeStruct((R, C), F32)
    grid_spec = pltpu.PrefetchScalarGridSpec(num_scalar_prefetch=1, grid=(2 * nbh,),
                                             in_specs=[blk, hblk, hblk, blk, blk], out_specs=[blk] * 4)
    return _pcall(body, name=name, grid_spec=grid_spec, out_shape=[shp] * 4,
                  compiler_params=_params("parallel"))(core, w, mine, theirs, m, v)


BIG = ("w_in", "w_dil_out", "w_fox_out", "w_out", "w_ffn_in", "w_ffn_down")
SMALL = ("norm_mix_g", "b_fgt", "b_gate", "norm_ffn_g", "norm_final_g")
ORDER = ("norm_mix_g", "w_in", "b_fgt", "b_gate", "w_dil_out", "w_fox_out", "w_out", "norm_ffn_g", "w_ffn_in",
         "w_ffn_down", "norm_final_g")
SMALL_ROWS = {"norm_mix_g": (0, 1), "b_gate": (1, 3), "norm_ffn_g": (3, 4), "norm_final_g": (4, 5), "b_fgt": (5, 6)}


def _columns_to_blocks(full, ncol):
    K = full.shape[0]
    return full.reshape(K, 4, ncol).transpose(1, 0, 2)


def _blocks_to_columns(blocks):
    n, K, ncol = blocks.shape
    return blocks.transpose(1, 0, 2).reshape(K, n * ncol)


def kernel(x, norm_mix_g, w_in, b_fgt, b_gate, w_dil_out, w_fox_out, w_out, norm_ffn_g, w_ffn_in, w_ffn_down, norm_final_g, loss_target, m_norm_mix_g, m_w_in, m_b_fgt, m_b_gate, m_w_dil_out, m_w_fox_out, m_w_out, m_norm_ffn_g, m_w_ffn_in, m_w_ffn_down, m_norm_final_g, v_norm_mix_g, v_w_in, v_b_fgt, v_b_gate, v_w_dil_out, v_w_fox_out, v_w_out, v_norm_ffn_g, v_w_ffn_in, v_w_ffn_down, v_norm_final_g):
    weights = dict(norm_mix_g=norm_mix_g, w_in=w_in, b_fgt=b_fgt, b_gate=b_gate, w_dil_out=w_dil_out,
                   w_fox_out=w_fox_out, w_out=w_out, norm_ffn_g=norm_ffn_g, w_ffn_in=w_ffn_in, w_ffn_down=w_ffn_down,
                   norm_final_g=norm_final_g)
    m_in = dict(norm_mix_g=m_norm_mix_g, w_in=m_w_in, b_fgt=m_b_fgt, b_gate=m_b_gate, w_dil_out=m_w_dil_out,
                w_fox_out=m_w_fox_out, w_out=m_w_out, norm_ffn_g=m_norm_ffn_g, w_ffn_in=m_w_ffn_in,
                w_ffn_down=m_w_ffn_down, norm_final_g=m_norm_final_g)
    v_in = dict(norm_mix_g=v_norm_mix_g, w_in=v_w_in, b_fgt=v_b_fgt, b_gate=v_b_gate, w_dil_out=v_w_dil_out,
                w_fox_out=v_w_fox_out, w_out=v_w_out, norm_ffn_g=v_norm_ffn_g, w_ffn_in=v_w_ffn_in,
                w_ffn_down=v_w_ffn_down, norm_final_g=v_norm_final_g)
    c = lax.axis_index("c")
    chip = 2 * lax.axis_index("x") + lax.axis_index("y")

    shards = {n: weights[n][0].astype(_CD) for n in BIG}
    in_shape = jax.ShapeDtypeStruct((4,) + shards["w_in"].shape, _CD)
    send_i, recv_i, in_src, in_land, token_in = _split_copy_start(
        [shards["w_in"]], [in_shape], _gather_copies, norm_mix_g, name="gather_in_start")
    late = BIG[1:]
    send_g, recv_g, late_src, late_land, token = _split_copy_start(
        [shards[n] for n in late], [jax.ShapeDtypeStruct((4,) + shards[n].shape, _CD) for n in late],
        _gather_whole_copies, token_in, name="gather_late_start")
    adam_in = [t[0] + token_in[0, 0] for t in (w_in, m_w_in, v_w_in)]
    p = dict(norm_mix_g=norm_mix_g, b_fgt=jnp.pad(b_fgt, ((0, 0), (0, F_PAD - N_FOX_HEADS))), b_gate=b_gate,
             norm_ffn_g=norm_ffn_g, norm_final_g=norm_final_g.reshape(1, D_MODEL))

    def first_weights(after):
        own, lands = _split_copy_wait(send_i, recv_i, in_src, in_land, _gather_copies, [after] + adam_in,
                                      name="gather_in_wait")
        (g_in,) = _forward_halves(lands, name="gather_in_forward")
        full_in = _blocks_to_columns(lax.dynamic_update_index_in_dim(g_in, own[0], chip, 0))
        o3 = QKV_COLS
        o4 = o3 + N_FOX_HEADS
        return dict(qkv=full_in[:, :o3], f=jnp.pad(full_in[:, o3:o4], ((0, 0), (0, F_PAD - N_FOX_HEADS))),
                    g=full_in[:, o4:])

    def late_weights(after):
        own, lands = _split_copy_wait(send_g, recv_g, late_src, late_land, _gather_whole_copies, after,
                                      name="gather_late_wait")
        g_dil, g_fox, g_out, g_ffn_in, g_ffn_down = [
            lax.dynamic_update_index_in_dim(l, s, chip, 0) for l, s in zip(lands, own)]
        return dict(dil_out=_blocks_to_columns(g_dil), fox_out=_blocks_to_columns(g_fox),
                    out=g_out.reshape(D_MODEL, D_MODEL), ffn_in=g_ffn_in,
                    ffn_down=g_ffn_down.reshape(D_FF, D_MODEL))

    def to_blocks(n, full):
        shape = weights[n].shape
        if full.ndim == 3:
            return full
        if n in ("w_out", "w_ffn_down"):
            return full.reshape(4, shape[1], shape[2])
        return _columns_to_blocks(full, shape[2])

    in_flight = {}

    def grad_sink(group, gw):
        if group == "in":
            named = {"w_in": jnp.concatenate([gw["qkv"], gw["f"][:, :N_FOX_HEADS], gw["g"]], axis=1)}
        else:
            named = {"w_" + k: v for k, v in gw.items()}
        srcs = [to_blocks(n, named[n]) for n in named]
        lands = [jax.ShapeDtypeStruct((7, s.shape[1] // 2, s.shape[2]), s.dtype) for s in srcs]
        started = _split_copy_start(srcs, lands, _scatter_all_copies, next(iter(gw.values())),
                                    name=f"scatter_{group}_start")
        in_flight[group] = (list(named), started)
        return started[-1]

    loss_part, grad_x, gw, small = _layer_step(x[0], loss_target[0], {}, p, late_weights, grad_sink,
                                               (token_in, token), first_weights)

    halves = {}
    where = jnp.stack([chip, c]).astype(jnp.int32)
    for group, (names, (send_s, recv_s, srcs, lands, _)) in in_flight.items():
        srcs, recv = _split_copy_wait(send_s, recv_s, srcs, lands, _scatter_all_copies, grad_x,
                                      name=f"scatter_{group}_wait")
        halves.update({n: _add_all(s, r, where, name=f"add_all_{n}") for n, s, r in zip(names, srcs, recv)})
    halves = [halves[n] for n in BIG]
    grad_halves = dict(zip(BIG, zip(halves, _share_halves(halves))))

    packed = jnp.concatenate([
        small["norm_mix_g"], small["b_gate"].reshape(2, D_MODEL), small["norm_ffn_g"], small["norm_final_g"],
        jnp.pad(small["b_fgt"], ((0, 0), (0, D_MODEL - F_PAD))), jnp.pad(loss_part, ((0, 0), (0, D_MODEL - 1))),
        jnp.zeros((1, D_MODEL), F32)], axis=0)
    summed = _sum_small(packed)
    loss = summed[6, 0]

    out_g, out_d, out_m, out_v = {}, {}, {}, {}
    core = jnp.reshape(c, (1,)).astype(jnp.int32)
    for n in SMALL:
        lo, hi = SMALL_ROWS[n]
        shape = weights[n].shape
        g2 = summed[lo:hi].reshape(1, -1)[:, :weights[n].size]
        d2, m2, v2 = _adamw(weights[n].reshape(g2.shape), g2, m_in[n].reshape(g2.shape), v_in[n].reshape(g2.shape),
                            name=f"adamw_{n}")
        out_g[n], out_d[n], out_m[n], out_v[n] = [t.reshape(shape) for t in (g2, d2, m2, v2)]
    for n in BIG:
        shape = weights[n].shape
        wmv = adam_in if n == "w_in" else [t[0] for t in (weights[n], m_in[n], v_in[n])]
        mine, theirs = grad_halves[n]
        outs = _adamw_halves(wmv[0], mine, theirs, wmv[1], wmv[2], core, name=f"adamw_{n}")
        out_g[n], out_d[n], out_m[n], out_v[n] = [t.reshape(shape) for t in outs]
    return (loss, grad_x[None], *[out_g[n] for n in ORDER], *[out_d[n] for n in ORDER],
            *[out_m[n] for n in ORDER], *[out_v[n] for n in ORDER])
```

```python
import numpy as np
import jax
import jax.numpy as jnp
from jax import lax
from jax.experimental import pallas as pl
from jax.experimental.pallas import tpu as pltpu

F32 = jnp.float32
_CD = jnp.bfloat16

D_MODEL = 1024
HEAD_DIM = 64
DIL_PAIRS = ((128, 1), (512, 4), (2048, 16))
N_DIL_GROUPS = 3
DIL_HEADS = 4
DIL_W = 128
DIL_OUT = DIL_HEADS * HEAD_DIM
DIL_WIDTH = N_DIL_GROUPS * DIL_OUT
N_FOX_HEADS = 8
FOX_WIDTH = N_FOX_HEADS * HEAD_DIM
D_FF = 2816
QKV_COLS = 3 * DIL_WIDTH + 3 * FOX_WIDTH
F_PAD = 128
RMS_EPS = 1e-6
NEG_INF = -1e30
ATTN_SCALE = HEAD_DIM ** -0.5
ADAM_LR, ADAM_B1, ADAM_B2, ADAM_EPS, ADAM_WD, ADAM_STEP = 0.001, 0.9, 0.999, 1e-08, 0.01, 10

VMEM_LIMIT = 48 * 1024 * 1024
VMEM_LIMIT_RESIDENT = 56 * 1024 * 1024
LANES = 128
MESH = pl.DeviceIdType.MESH
HBM_SPEC = pl.BlockSpec(memory_space=pltpu.HBM)


def _pcall(body, after=None, **kw):
    if after is None:
        return pl.pallas_call(body, **kw)
    n_in = len(kw["in_specs"])
    kw["in_specs"] = list(kw["in_specs"]) + [pl.BlockSpec(memory_space=pl.ANY)]

    def tied(*refs):
        return body(*refs[:n_in], *refs[n_in + 1:])

    call = pl.pallas_call(tied, **kw)
    return lambda *args: call(*args, after)


def _params(*sem):
    return pltpu.CompilerParams(dimension_semantics=sem, vmem_limit_bytes=VMEM_LIMIT)


def _pick(dim, pref):
    t = (min(pref, dim) // 128) * 128
    while t >= 128:
        if dim % t == 0:
            return t
        t -= 128
    return dim


def _mm(a, b, *, name, ta=False, tb=False, out_dtype=F32, add=None, tm=1024, tn=512, tk=2048, after=None,
        b_blocks=False, out_blocks=None, a_halves=False, b_halves=False, rms_bwd=None, more=()):
    if a_halves:
        M, K = a.shape[1], 2 * a.shape[2]
    elif ta:
        K, M = a.shape
    else:
        M, K = a.shape
    if b_halves:
        b_rows, b_cols = b.shape[1], 2 * b.shape[2]
    else:
        b_rows, b_cols = (b.shape[1], b.shape[0] * b.shape[2]) if b_blocks else b.shape
    if tb:
        N, K2 = b_rows, b_cols
    else:
        K2, N = b_rows, b_cols
    assert K == K2, (a.shape, b.shape)
    shard = b.shape[2] if b_blocks else None
    tm = _pick(M, tm)
    tn = _pick(shard if (b_blocks and not tb) else (out_blocks or N), tn)
    tk = _pick(shard if (b_blocks and tb) else K, tk)
    nk = K // tk
    dn = (((0 if ta else 1,), (1 if tb else 0,)), ((), ()))
    has_add = add is not None
    assert not (has_add and out_blocks)
    has_norm = rms_bwd is not None
    if has_norm:
        tn = N
        assert not out_blocks and out_dtype == F32
    assert not more or (nk == 1 and tb and not ta)

    def body(*refs):
        a_ref, b_ref = refs[0], refs[1]
        rest = list(refs[2:])
        more_refs = [(rest.pop(0), rest.pop(0)) for _ in more]
        add_ref = rest.pop(0) if has_add else None
        x_ref, g_ref, dres_ref = (rest.pop(0), rest.pop(0), rest.pop(0)) if has_norm else (None, None, None)
        o_ref = rest.pop(0)
        dg_ref = rest.pop(0) if has_norm else None
        bv = b_ref[0] if b_blocks else b_ref[...]
        p = lax.dot_general(a_ref[...].astype(_CD), bv.astype(_CD), dn, preferred_element_type=F32)
        for a2_ref, b2_ref in more_refs:
            p += lax.dot_general(a2_ref[...].astype(_CD), b2_ref[...].astype(_CD), dn, preferred_element_type=F32)

        def finish(r):
            if has_add:
                r = r + add_ref[...]
            if has_norm:
                xv = x_ref[...]
                rs = lax.rsqrt(jnp.mean(xv * xv, axis=-1, keepdims=True) + RMS_EPS)
                xh = xv * rs
                dxh = r * g_ref[...]
                o_ref[...] = dres_ref[...] + rs * (dxh - xh * jnp.mean(dxh * xh, axis=-1, keepdims=True))
                part = jnp.sum(r * xh, axis=0, keepdims=True)
                first = pl.program_id(0) == 0

                @pl.when(first)
                def _():
                    dg_ref[...] = part

                @pl.when(jnp.logical_not(first))
                def _():
                    dg_ref[...] += part
            elif out_blocks:
                o_ref[0] = r.astype(out_dtype)
            else:
                o_ref[...] = r.astype(out_dtype)

        if nk == 1:
            finish(p)
        else:
            acc_ref = rest.pop(0)
            k = pl.program_id(2)

            @pl.when(k == 0)
            def _():
                acc_ref[...] = p

            @pl.when(k > 0)
            def _():
                acc_ref[...] += p

            @pl.when(k == nk - 1)
            def _():
                finish(acc_ref[...])

    if a_halves:
        ka = (K // 2) // tk
        a_spec = pl.BlockSpec((None, tm, tk), lambda i, j, k: (k // ka, i, k % ka))
    else:
        a_spec = pl.BlockSpec((tk, tm), lambda i, j, k: (k, i)) if ta else pl.BlockSpec((tm, tk), lambda i, j, k: (i, k))
    if b_halves:
        nb_ = (N // 2) // tn
        b_spec = pl.BlockSpec((None, tk, tn), lambda i, j, k: (j // nb_, k, j % nb_))
    elif b_blocks and tb:
        per = shard // tk
        b_spec = pl.BlockSpec((1, tn, tk), lambda i, j, k: (k // per, j, k % per))
    elif b_blocks:
        per = shard // tn
        b_spec = pl.BlockSpec((1, tk, tn), lambda i, j, k: (j // per, k, j % per))
    else:
        b_spec = pl.BlockSpec((tn, tk), lambda i, j, k: (j, k)) if tb else pl.BlockSpec((tk, tn), lambda i, j, k: (k, j))
    if out_blocks:
        oper = out_blocks // tn
        o_spec = pl.BlockSpec((1, tm, tn), lambda i, j, k: (j // oper, i, j % oper))
        out_shape = jax.ShapeDtypeStruct((N // out_blocks, M, out_blocks), out_dtype)
    else:
        o_spec = pl.BlockSpec((tm, tn), lambda i, j, k: (i, j))
        out_shape = jax.ShapeDtypeStruct((M, N), out_dtype)
    in_specs, args = [a_spec, b_spec], (a, b)
    for a2, b2 in more:
        assert a2.shape[0] == M and b2.shape == (N, a2.shape[1]), (a2.shape, b2.shape)
        in_specs += [pl.BlockSpec((tm, a2.shape[1]), lambda i, j, k: (i, 0)),
                     pl.BlockSpec((tn, a2.shape[1]), lambda i, j, k: (j, 0))]
        args += (a2, b2)
    if has_add:
        in_specs, args = in_specs + [o_spec], args + (add,)
    out_specs, semantics = o_spec, ("parallel", "parallel", "arbitrary")
    if has_norm:
        vec = pl.BlockSpec((1, N), lambda i, j, k: (0, 0))
        in_specs += [o_spec, vec, o_spec]
        args += tuple(rms_bwd)
        out_specs, out_shape = [o_spec, vec], [out_shape, jax.ShapeDtypeStruct((1, N), F32)]
        semantics = ("arbitrary", "arbitrary", "arbitrary")
    return _pcall(
        body, after, name=name, grid=(M // tm, N // tn, nk), in_specs=in_specs, out_specs=out_specs,
        out_shape=out_shape,
        scratch_shapes=[pltpu.VMEM((tm, tn), F32)] if nk > 1 else [],
        compiler_params=_params(*semantics),
    )(*args)


def _rms_fwd(x, g, *, name, tm=512, after=None):
    S, D = x.shape

    def body(x_ref, g_ref, h_ref):
        xv = x_ref[...]
        r = lax.rsqrt(jnp.mean(xv * xv, axis=-1, keepdims=True) + RMS_EPS)
        h_ref[...] = ((xv * r) * g_ref[...]).astype(h_ref.dtype)

    row = pl.BlockSpec((tm, D), lambda i: (i, 0))
    return _pcall(body, after, name=name, grid=(S // tm,), in_specs=[row, pl.BlockSpec((1, D), lambda i: (0, 0))],
                  out_specs=row, out_shape=jax.ShapeDtypeStruct((S, D), _CD), compiler_params=_params("parallel"))(x, g)


def _ffn_down_loss(act, w_down, x1, g, tgt, *, name, tm=512):
    S, D = x1.shape
    F = act.shape[1]

    def body(a_ref, b_ref, x_ref, g_ref, t_ref, loss_ref, dx_ref, dg_ref):
        xv = x_ref[...] + jnp.dot(a_ref[...].astype(_CD), b_ref[...].astype(_CD), preferred_element_type=F32)
        gv = g_ref[...]
        r = lax.rsqrt(jnp.mean(xv * xv, axis=-1, keepdims=True) + RMS_EPS)
        xh = xv * r
        err = xh * gv - t_ref[...]
        lpart = 0.5 * jnp.sum(jnp.mean(err * err, axis=-1, keepdims=True), axis=0, keepdims=True)
        dy = err * (1.0 / D)
        dxh = dy * gv
        dx_ref[...] = r * (dxh - xh * jnp.mean(dxh * xh, axis=-1, keepdims=True))
        gpart = jnp.sum(dy * xh, axis=0, keepdims=True)

        @pl.when(pl.program_id(0) == 0)
        def _():
            loss_ref[...] = lpart
            dg_ref[...] = gpart

        @pl.when(pl.program_id(0) > 0)
        def _():
            loss_ref[...] += lpart
            dg_ref[...] += gpart

    row = pl.BlockSpec((tm, D), lambda i: (i, 0))
    vec = pl.BlockSpec((1, D), lambda i: (0, 0))
    one = pl.BlockSpec((1, 1), lambda i: (0, 0))
    return _pcall(body, name=name, grid=(S // tm,),
                  in_specs=[pl.BlockSpec((tm, F), lambda i: (i, 0)), pl.BlockSpec((F, D), lambda i: (0, 0)), row, vec, row],
                  out_specs=[one, row, vec],
                  out_shape=[jax.ShapeDtypeStruct((1, 1), F32), jax.ShapeDtypeStruct((S, D), F32),
                             jax.ShapeDtypeStruct((1, D), F32)],
                  compiler_params=_params("arbitrary"))(act, w_down, x1, g, tgt)


def _sigmoid(z):
    return 1.0 / (1.0 + jnp.exp(-z))


def _gated_mix_out(gl, bg, ya, yb, w_out, x, g, *, name, tm=512):
    S, D = ya.shape

    def body(za_ref, zb_ref, ba_ref, bb_ref, ya_ref, yb_ref, w_ref, x_ref, g_ref, m_ref, x1_ref, h_ref):
        ga = _sigmoid(za_ref[...].astype(F32) + ba_ref[...])
        gb = _sigmoid(zb_ref[...].astype(F32) + bb_ref[...])
        merged = (ga * ya_ref[...].astype(F32) + gb * yb_ref[...].astype(F32)).astype(m_ref.dtype)
        m_ref[...] = merged
        x1 = x_ref[...] + jnp.dot(merged, w_ref[...].astype(_CD), preferred_element_type=F32)
        x1_ref[...] = x1
        rs = lax.rsqrt(jnp.mean(x1 * x1, axis=-1, keepdims=True) + RMS_EPS)
        h_ref[...] = ((x1 * rs) * g_ref[...]).astype(h_ref.dtype)

    lo = pl.BlockSpec((tm, D), lambda i: (i, 0))
    hi = pl.BlockSpec((tm, D), lambda i: (i, 1))
    vlo = pl.BlockSpec((1, D), lambda i: (0, 0))
    vhi = pl.BlockSpec((1, D), lambda i: (0, 1))
    whole = pl.BlockSpec((D, D), lambda i: (0, 0))
    return _pcall(body, name=name, grid=(S // tm,), in_specs=[lo, hi, vlo, vhi, lo, lo, whole, lo, vlo],
                  out_specs=[lo, lo, lo],
                  out_shape=[jax.ShapeDtypeStruct((S, D), _CD), jax.ShapeDtypeStruct((S, D), F32),
                             jax.ShapeDtypeStruct((S, D), _CD)],
                  compiler_params=_params("parallel"))(gl, gl, bg, bg, ya, yb, w_out, x, g)


def _gate_bwd(dx1, w_out, gl, bg, ya, yb, *, name, tm=512):
    S, D = ya.shape
    nt = (((1,), (1,)), ((), ()))

    def body(dx_ref, w_ref, za_ref, zb_ref, ba_ref, bb_ref, ya_ref, yb_ref, dya_ref, dyb_ref, dgl_ref, dbg_ref):
        dmv = lax.dot_general(dx_ref[...].astype(_CD), w_ref[...].astype(_CD), nt, preferred_element_type=F32)
        ga = _sigmoid(za_ref[...].astype(F32) + ba_ref[...])
        gb = _sigmoid(zb_ref[...].astype(F32) + bb_ref[...])
        dya_ref[...] = (dmv * ga).astype(dya_ref.dtype)
        dyb_ref[...] = (dmv * gb).astype(dyb_ref.dtype)
        dza = dmv * ya_ref[...].astype(F32) * ga * (1.0 - ga)
        dzb = dmv * yb_ref[...].astype(F32) * gb * (1.0 - gb)
        dgl_ref[:, :D] = dza.astype(dgl_ref.dtype)
        dgl_ref[:, D:] = dzb.astype(dgl_ref.dtype)
        pa = jnp.sum(dza, axis=0, keepdims=True)
        pb = jnp.sum(dzb, axis=0, keepdims=True)

        @pl.when(pl.program_id(0) == 0)
        def _():
            dbg_ref[:, :D] = pa
            dbg_ref[:, D:] = pb

        @pl.when(pl.program_id(0) > 0)
        def _():
            dbg_ref[:, :D] += pa
            dbg_ref[:, D:] += pb

    lo = pl.BlockSpec((tm, D), lambda i: (i, 0))
    hi = pl.BlockSpec((tm, D), lambda i: (i, 1))
    vlo = pl.BlockSpec((1, D), lambda i: (0, 0))
    vhi = pl.BlockSpec((1, D), lambda i: (0, 1))
    wide = pl.BlockSpec((tm, 2 * D), lambda i: (i, 0))
    vwide = pl.BlockSpec((1, 2 * D), lambda i: (0, 0))
    whole = pl.BlockSpec((D, D), lambda i: (0, 0))
    return _pcall(body, name=name, grid=(S // tm,), in_specs=[lo, whole, lo, hi, vlo, vhi, lo, lo],
                  out_specs=[lo, lo, wide, vwide],
                  out_shape=[jax.ShapeDtypeStruct((S, D), _CD), jax.ShapeDtypeStruct((S, D), _CD),
                             jax.ShapeDtypeStruct((S, 2 * D), _CD), jax.ShapeDtypeStruct((1, 2 * D), F32)],
                  compiler_params=_params("arbitrary"))(dx1, w_out, gl, gl, bg, bg, ya, yb)


def _ffn_in_act(h2, w_blocks, *, name, tm=512):
    S, D = h2.shape
    _, _, C = w_blocks.shape

    def body(a_ref, bg_ref, bu_ref, g_ref, u_ref, o_ref):
        av = a_ref[...].astype(_CD)
        gv = jnp.dot(av, bg_ref[0].astype(_CD), preferred_element_type=F32)
        uv = jnp.dot(av, bu_ref[0].astype(_CD), preferred_element_type=F32)
        g_ref[...] = gv.astype(g_ref.dtype)
        u_ref[...] = uv.astype(u_ref.dtype)
        o_ref[...] = (gv * _sigmoid(gv) * uv).astype(o_ref.dtype)

    out = pl.BlockSpec((tm, C), lambda i, j: (i, j))
    shp = jax.ShapeDtypeStruct((S, 2 * C), _CD)
    return _pcall(body, name=name, grid=(S // tm, 2),
                  in_specs=[pl.BlockSpec((tm, D), lambda i, j: (i, 0)), pl.BlockSpec((1, D, C), lambda i, j: (j, 0, 0)),
                            pl.BlockSpec((1, D, C), lambda i, j: (2 + j, 0, 0))],
                  out_specs=[out, out, out], out_shape=[shp, shp, shp],
                  compiler_params=_params("parallel", "arbitrary"))(h2, w_blocks, w_blocks)


def _d_swiglu(dx, w_down, gate, up, *, name, tm=512, tn=1408):
    S, D = dx.shape
    F = w_down.shape[0]
    nt = (((1,), (1,)), ((), ()))

    def body(a_ref, b_ref, g_ref, u_ref, o_ref):
        dv = lax.dot_general(a_ref[...].astype(_CD), b_ref[...].astype(_CD), nt, preferred_element_type=F32)
        gv = g_ref[...].astype(F32)
        sg = _sigmoid(gv)
        o_ref[0] = (dv * u_ref[...].astype(F32) * (sg * (1.0 + gv * (1.0 - sg)))).astype(o_ref.dtype)
        o_ref[1] = (dv * (gv * sg)).astype(o_ref.dtype)

    tile = pl.BlockSpec((tm, tn), lambda i, j: (i, j))
    return _pcall(body, name=name, grid=(S // tm, F // tn),
                  in_specs=[pl.BlockSpec((tm, D), lambda i, j: (i, 0)), pl.BlockSpec((tn, D), lambda i, j: (j, 0)),
                            tile, tile],
                  out_specs=pl.BlockSpec((2, tm, tn), lambda i, j: (0, i, j)),
                  out_shape=jax.ShapeDtypeStruct((2, S, F), _CD),
                  compiler_params=_params("parallel", "arbitrary"))(dx, w_down, gate, up)


def _split3(x):
    hi = x.astype(jnp.bfloat16)
    r1 = x - hi.astype(F32)
    mid = r1.astype(jnp.bfloat16)
    lo = (r1 - mid.astype(F32)).astype(jnp.bfloat16)
    return hi, mid, lo


def _ones_dot_left(ones, x):
    return sum(jnp.dot(ones, p, preferred_element_type=F32) for p in _split3(x))


def _ones_dot_right(x, ones):
    return sum(jnp.dot(p, ones, preferred_element_type=F32) for p in _split3(x))


def _head_sum(x):
    n = x.shape[1]
    r = lax.broadcasted_iota(jnp.int32, (n, n), 0) // HEAD_DIM
    c = lax.broadcasted_iota(jnp.int32, (n, n), 1) // HEAD_DIM
    return _ones_dot_right(x, (r == c).astype(jnp.bfloat16))


def _log_sigmoid(z):
    e = jnp.exp(-jnp.abs(z))
    t = 1.0 + e
    log1p_e = jnp.where(t == 1.0, e, jnp.log(t) * (e / jnp.where(t == 1.0, 1.0, t - 1.0)))
    return jnp.minimum(z, 0.0) - log1p_e


def _fox_cumsum(zf, bf, *, name):
    S, W = zf.shape
    nb = S // 128

    def body(z_ref, b_ref, c_ref):
        tri = (lax.broadcasted_iota(jnp.int32, (128, 128), 0) >= lax.broadcasted_iota(jnp.int32, (128, 128), 1))
        tri = tri.astype(jnp.bfloat16)

        def step(i, carry):
            rows = pl.ds(pl.multiple_of(i * 128, 128), 128)
            lf = _log_sigmoid(z_ref[rows, :] + b_ref[...])
            cb = _ones_dot_left(tri, lf) + carry
            c_ref[rows, :] = cb
            return cb[127:128, :]

        lax.fori_loop(0, nb, step, jnp.zeros((1, W), F32))

    return _pcall(body, name=name, out_shape=jax.ShapeDtypeStruct((S, W), F32),
                  compiler_params=pltpu.CompilerParams(vmem_limit_bytes=VMEM_LIMIT))(zf, bf)


def _fox_cumsum_bwd(dc, zf, bf, *, name):
    S, W = zf.shape
    nb = S // 128

    def body(dc_ref, z_ref, b_ref, dz_ref, db_ref):
        tri = (lax.broadcasted_iota(jnp.int32, (128, 128), 0) <= lax.broadcasted_iota(jnp.int32, (128, 128), 1))
        tri = tri.astype(jnp.bfloat16)

        def step(k, carry):
            tail, acc = carry
            i = nb - 1 - k
            rows = pl.ds(pl.multiple_of(i * 128, 128), 128)
            dlf = _ones_dot_left(tri, dc_ref[rows, :]) + tail
            dz = dlf * _sigmoid(-(z_ref[rows, :] + b_ref[...]))
            dz_ref[rows, :] = dz
            return dlf[0:1, :], acc + jnp.sum(dz, axis=0, keepdims=True)

        _, acc = lax.fori_loop(0, nb, step, (jnp.zeros((1, W), F32), jnp.zeros((1, W), F32)))
        db_ref[...] = acc

    return _pcall(body, name=name,
                  out_shape=[jax.ShapeDtypeStruct((S, W), F32), jax.ShapeDtypeStruct((1, W), F32)],
                  compiler_params=pltpu.CompilerParams(vmem_limit_bytes=VMEM_LIMIT))(dc, zf, bf)


def _proj_dil(h, w_qkv, *, name, tm=1024):
    S, D = h.shape
    tn = DIL_WIDTH

    def body(a_ref, b_ref, *rest):
        outs, acc = rest[:N_DIL_GROUPS], rest[N_DIL_GROUPS]
        prod = jnp.dot(a_ref[...].astype(_CD), b_ref[...].astype(_CD), preferred_element_type=F32)
        for k in range(tn // LANES):
            acc[k] = prod[:, k * LANES:(k + 1) * LANES]
        for g, (_, d) in enumerate(DIL_PAIRS):
            for half in range(DIL_OUT // LANES):
                k = g * (DIL_OUT // LANES) + half
                cols = slice(half * LANES, (half + 1) * LANES)
                for r in range(d):
                    rows = pl.ds(r, tm // d, stride=d) if d > 1 else slice(None)
                    outs[g][0, r, :, cols] = acc[k, rows, :].astype(outs[g].dtype)

    out_specs = [pl.BlockSpec((1, d, tm // d, DIL_OUT), lambda i, j: (j, 0, i, 0)) for _, d in DIL_PAIRS]
    out_shape = [jax.ShapeDtypeStruct((3, d, S // d, DIL_OUT), _CD) for _, d in DIL_PAIRS]
    outs = _pcall(body, name=name, grid=(S // tm, 3),
                  in_specs=[pl.BlockSpec((tm, D), lambda i, j: (i, 0)), pl.BlockSpec((D, tn), lambda i, j: (0, j))],
                  out_specs=out_specs, out_shape=out_shape, scratch_shapes=[pltpu.VMEM((tn // LANES, tm, LANES), F32)],
                  compiler_params=_params("parallel", "arbitrary"))(h, w_qkv)
    return [o.reshape(3, S, DIL_OUT) for o in outs]


def _dil_start(block, S, dilation):
    sub = S // dilation
    u0 = block * DIL_W
    return (u0 % sub) * dilation + u0 // sub


def _dil_slopes(group):
    h = np.arange(1, N_DIL_GROUPS * DIL_HEADS + 1, dtype=np.float32)
    s = (np.float32(2.0) ** (np.float32(-8.0) * h / np.float32(N_DIL_GROUPS * DIL_HEADS))).astype(np.float32)
    return [float(v) for v in s.reshape(N_DIL_GROUPS, DIL_HEADS)[group]]


def _dil_tiles(i, n, blocks_per_seq):
    qi = lax.broadcasted_iota(jnp.int32, (DIL_W, 2 * DIL_W), 0)
    kj = lax.broadcasted_iota(jnp.int32, (DIL_W, 2 * DIL_W), 1)
    rel = qi + DIL_W - kj
    first = ((4 * n + i) % blocks_per_seq) == 0
    valid = jnp.logical_and(jnp.logical_and(rel >= 0, rel <= DIL_W), jnp.logical_or(kj >= DIL_W, jnp.logical_not(first)))
    return valid, rel.astype(F32)


def _dil_window(cur_ref, prev_ref, i, cols):
    if i > 0:
        return cur_ref[(i - 1) * DIL_W:(i + 1) * DIL_W, cols]
    return jnp.concatenate([prev_ref[:, cols], cur_ref[:DIL_W, cols]], axis=0)


CHUNK = 4 * DIL_W


def _dil_rows(block, S, dilation):
    start = _dil_start(block, S, dilation)
    return pl.ds(start, DIL_W, stride=dilation) if dilation > 1 else pl.ds(start, DIL_W)


def SPLIT(S):
    return (DIL_OUT // LANES, S, LANES)


def _dil_fwd(qkv, group, *, name):
    S = qkv.shape[1]
    dilation = DIL_PAIRS[group][1]
    bps = (S // dilation) // DIL_W
    slopes = _dil_slopes(group)
    nt = (((1,), (1,)), ((), ()))

    def body(q_ref, k_ref, v_ref, kp_ref, vp_ref, on_ref, ln_ref, o_ref, l_ref):
        n = pl.program_id(0)
        for i in range(4):
            valid, rel = _dil_tiles(i, n, bps)
            rows = slice(i * DIL_W, (i + 1) * DIL_W)
            for h in range(DIL_HEADS):
                cols = slice(h * HEAD_DIM, (h + 1) * HEAD_DIM)
                qh = q_ref[rows, cols]
                k2, v2 = _dil_window(k_ref, kp_ref, i, cols), _dil_window(v_ref, vp_ref, i, cols)
                s = lax.dot_general(qh, k2, nt, preferred_element_type=F32) * ATTN_SCALE - (slopes[h] * dilation) * rel
                s = jnp.where(valid, s, NEG_INF)
                m = jnp.max(s, axis=-1, keepdims=True)
                p = jnp.exp(s - m)
                den = jnp.sum(p, axis=-1, keepdims=True)
                acc = jnp.dot(p.astype(_CD), v2, preferred_element_type=F32)
                o_ref[rows, cols] = acc / den
                l_ref[rows, cols] = jnp.broadcast_to(m + jnp.log(den), (DIL_W, HEAD_DIM))
        for i in range(4):
            rows = slice(i * DIL_W, (i + 1) * DIL_W)
            nat = _dil_rows(4 * n + i, S, dilation)
            for half in range(DIL_OUT // LANES):
                cols = slice(half * LANES, (half + 1) * LANES)
                on_ref[half, nat, :] = o_ref[rows, cols]
                ln_ref[half, nat, :] = l_ref[rows, cols]

    def cur(which):
        return pl.BlockSpec((None, CHUNK, DIL_OUT), lambda n: (which, n, 0))

    def prev(which):
        return pl.BlockSpec((None, DIL_W, DIL_OUT), lambda n: (which, jnp.maximum(4 * n - 1, 0), 0))

    whole = pl.BlockSpec(SPLIT(S), lambda n: (0, 0, 0))
    return _pcall(body, name=name, grid=(S // CHUNK,), in_specs=[cur(0), cur(1), cur(2), prev(1), prev(2)],
                  out_specs=[whole, whole],
                  out_shape=[jax.ShapeDtypeStruct(SPLIT(S), F32), jax.ShapeDtypeStruct(SPLIT(S), F32)],
                  scratch_shapes=[pltpu.VMEM((CHUNK, DIL_OUT), F32), pltpu.VMEM((CHUNK, DIL_OUT), F32)],
                  compiler_params=_params("arbitrary"))(qkv, qkv, qkv, qkv, qkv)


STAT_OFFSET = HEAD_DIM // 2


def _dil_bwd(qkv, stats, do, group, *, name):
    S = qkv.shape[1]
    dilation = DIL_PAIRS[group][1]
    bps = (S // dilation) // DIL_W
    slopes = _dil_slopes(group)
    nchunk = S // CHUNK
    nt = (((1,), (1,)), ((), ()))
    tn = (((0,), (0,)), ((), ()))

    def body(q_ref, k_ref, v_ref, kp_ref, vp_ref, ln_ref, don_ref, dqn_ref, dkn_ref, dvn_ref,
             dk_s, dv_s, l_ref, do_ref, dq_ref):
        step = pl.program_id(0)
        n = nchunk - 1 - step
        for i in range(4):
            rows = slice(i * DIL_W, (i + 1) * DIL_W)
            nat = _dil_rows(4 * n + i, S, dilation)
            for half in range(DIL_OUT // LANES):
                cols = slice(half * LANES, (half + 1) * LANES)
                l_ref[rows, cols] = ln_ref[half, nat, :]
                do_ref[rows, cols] = don_ref[half, nat, :]

        @pl.when(step == 0)
        def _():
            dk_s[:, CHUNK:] = jnp.zeros((DIL_OUT, DIL_W), F32)
            dv_s[:, CHUNK:] = jnp.zeros((DIL_OUT, DIL_W), F32)

        dk_s[:, :CHUNK] = jnp.zeros((DIL_OUT, CHUNK), F32)
        dv_s[:, :CHUNK] = jnp.zeros((DIL_OUT, CHUNK), F32)
        for i in range(4):
            valid, rel = _dil_tiles(i, n, bps)
            rows = slice(i * DIL_W, (i + 1) * DIL_W)
            window = slice(i * DIL_W, (i + 2) * DIL_W)
            for h in range(DIL_HEADS):
                cols = slice(h * HEAD_DIM, (h + 1) * HEAD_DIM)
                qh = q_ref[rows, cols]
                k2, v2 = _dil_window(k_ref, kp_ref, i, cols), _dil_window(v_ref, vp_ref, i, cols)
                lh = l_ref[rows, h * HEAD_DIM:h * HEAD_DIM + 1]
                shift = l_ref[rows, h * HEAD_DIM + STAT_OFFSET:h * HEAD_DIM + STAT_OFFSET + 1]
                s = lax.dot_general(qh, k2, nt, preferred_element_type=F32) * ATTN_SCALE - (slopes[h] * dilation) * rel
                p = jnp.exp(jnp.where(valid, s, NEG_INF) - lh)
                dob = do_ref[rows, cols].astype(_CD)
                ds = p * (lax.dot_general(dob, v2, nt, preferred_element_type=F32) + shift)
                dsb = (ds * ATTN_SCALE).astype(_CD)
                dq_ref[rows, cols] = jnp.dot(dsb, k2, preferred_element_type=F32)
                dk_s[cols, window] += lax.dot_general(qh, dsb, tn, preferred_element_type=F32)
                dv_s[cols, window] += lax.dot_general(dob, p.astype(_CD), tn, preferred_element_type=F32)
        for i in range(4):
            rows = slice(i * DIL_W, (i + 1) * DIL_W)
            done = slice((i + 1) * DIL_W, (i + 2) * DIL_W)
            nat = _dil_rows(4 * n + i, S, dilation)
            dkb, dvb = dk_s[:, done].T, dv_s[:, done].T
            for half in range(DIL_OUT // LANES):
                cols = slice(half * LANES, (half + 1) * LANES)
                dqn_ref[half, nat, :] = dq_ref[rows, cols]
                dkn_ref[half, nat, :] = dkb[:, cols]
                dvn_ref[half, nat, :] = dvb[:, cols]
        dk_s[:, CHUNK:] = dk_s[:, :DIL_W]
        dv_s[:, CHUNK:] = dv_s[:, :DIL_W]

    def cur(which):
        return pl.BlockSpec((None, CHUNK, DIL_OUT), lambda s: (which, nchunk - 1 - s, 0))

    def prev(which):
        return pl.BlockSpec((None, DIL_W, DIL_OUT), lambda s: (which, jnp.maximum(4 * (nchunk - 1 - s) - 1, 0), 0))

    whole = pl.BlockSpec(SPLIT(S), lambda s: (0, 0, 0))
    shp = jax.ShapeDtypeStruct(SPLIT(S), F32)
    tile = pltpu.VMEM((CHUNK, DIL_OUT), F32)
    return _pcall(body, name=name, grid=(nchunk,),
                  in_specs=[cur(0), cur(1), cur(2), prev(1), prev(2), whole, whole],
                  out_specs=[whole, whole, whole], out_shape=[shp, shp, shp],
                  scratch_shapes=[pltpu.VMEM((DIL_OUT, CHUNK + DIL_W), F32), pltpu.VMEM((DIL_OUT, CHUNK + DIL_W), F32),
                                  tile, tile, tile],
                  compiler_params=pltpu.CompilerParams(dimension_semantics=("arbitrary",),
                                                       vmem_limit_bytes=VMEM_LIMIT_RESIDENT))(
        qkv, qkv, qkv, qkv, qkv, stats, do)


def _dil_mix_fwd(os_, ls_, *, name, tm=512):
    nh, S, _ = os_[0].shape

    def body(o0, o1, o2, l0, l1, l2, out_ref):
        for half in range(nh):
            ls = [l0[half], l1[half], l2[half]]
            m = jnp.maximum(jnp.maximum(ls[0], ls[1]), ls[2])
            es = [jnp.exp(l - m) for l in ls]
            den = es[0] + es[1] + es[2]
            mixed = (es[0] * o0[half] + es[1] * o1[half] + es[2] * o2[half]) / den
            out_ref[:, half * LANES:(half + 1) * LANES] = mixed.astype(out_ref.dtype)

    halves = pl.BlockSpec((nh, tm, LANES), lambda i: (0, i, 0))
    row = pl.BlockSpec((tm, nh * LANES), lambda i: (i, 0))
    return _pcall(body, name=name, grid=(S // tm,), in_specs=[halves] * 6, out_specs=row,
                  out_shape=jax.ShapeDtypeStruct((S, nh * LANES), _CD), compiler_params=_params("parallel"))(*os_, *ls_)


def _dil_mix_bwd(doa, os_, ls_, *, name, tm=512, after=None):
    nh, S, _ = os_[0].shape

    def body(d_ref, o0, o1, o2, l0, l1, l2, do0, do1, do2, st0, st1, st2):
        first = lax.broadcasted_iota(jnp.int32, (tm, LANES), 1) % HEAD_DIM < STAT_OFFSET
        for half in range(nh):
            dv = d_ref[:, half * LANES:(half + 1) * LANES]
            ls = [l0[half], l1[half], l2[half]]
            m = jnp.maximum(jnp.maximum(ls[0], ls[1]), ls[2])
            es = [jnp.exp(l - m) for l in ls]
            den = es[0] + es[1] + es[2]
            al = [e / den for e in es]
            da = [_head_sum(dv * o[half]) for o in (o0, o1, o2)]
            mean = al[0] * da[0] + al[1] * da[1] + al[2] * da[2]
            for a, l, do_ref, st_ref in zip(al, ls, (do0, do1, do2), (st0, st1, st2)):
                do_ref[half] = a * dv
                st_ref[half] = jnp.where(first, l, -a * mean)

    halves = pl.BlockSpec((nh, tm, LANES), lambda i: (0, i, 0))
    row = pl.BlockSpec((tm, nh * LANES), lambda i: (i, 0))
    shp = jax.ShapeDtypeStruct((nh, S, LANES), F32)
    return _pcall(body, after, name=name, grid=(S // tm,), in_specs=[row] + [halves] * 6, out_specs=[halves] * 6,
                  out_shape=[shp] * 6, compiler_params=_params("parallel"))(doa, *os_, *ls_)


FOX_T = 512


PACK = 2 * HEAD_DIM
HEAD_PAIRS = N_FOX_HEADS // 2
FOX_HPS = 8
Q_BLOCK0 = 0
K_BLOCK0 = FOX_WIDTH // PACK
V_BLOCK0 = 2 * FOX_WIDTH // PACK


def _pieces(x):
    hi = x.astype(jnp.bfloat16).astype(F32)
    r = x - hi
    mid = r.astype(jnp.bfloat16).astype(F32)
    lo = (r - mid).astype(jnp.bfloat16).astype(F32)
    return [hi, mid, lo]


def _extras(first, second, rows):
    lane = lax.broadcasted_iota(jnp.int32, (rows, HEAD_DIM), 1)
    out = jnp.zeros((rows, HEAD_DIM), F32)
    for base, triple in ((0, first), (3, second)):
        if all(isinstance(v, float) for v in triple) and len(set(triple)) == 1:
            if triple[0] != 0.0:
                out = jnp.where(jnp.logical_and(lane >= base, lane < base + 3), triple[0], out)
        else:
            for idx, val in enumerate(triple):
                out = jnp.where(lane == base + idx, val, out)
    return out


def _head_column(c, h):
    lane = lax.broadcasted_iota(jnp.int32, c.shape, 1)
    return jnp.sum(jnp.where(lane == h, c, 0.0), axis=1, keepdims=True)


ONES3 = [1.0, 1.0, 1.0]
ZEROS3 = [0.0, 0.0, 0.0]


def _fox_pack_fwd(qkv, c, *, name, tm=1024):
    S = qkv.shape[0]

    def body(q_ref, k_ref, v_ref, c_ref, qo_ref, ko_ref, vo_ref):
        hp = pl.program_id(1)
        cv = c_ref[...]
        v_extras = jnp.where(lax.broadcasted_iota(jnp.int32, (tm, HEAD_DIM), 1) < 3, 1.0, 0.0).astype(vo_ref.dtype)
        for hh in range(2):
            ch = _pieces(_head_column(cv, 2 * hp + hh))
            src = slice(hh * HEAD_DIM, (hh + 1) * HEAD_DIM)
            lo = slice(hh * PACK, hh * PACK + HEAD_DIM)
            hi = slice(hh * PACK + HEAD_DIM, (hh + 1) * PACK)
            qo_ref[:, lo] = (q_ref[:, src].astype(F32) * ATTN_SCALE).astype(qo_ref.dtype)
            qo_ref[:, hi] = _extras(ch, ONES3, tm).astype(qo_ref.dtype)
            ko_ref[:, lo] = k_ref[:, src]
            ko_ref[:, hi] = _extras(ONES3, [-p for p in ch], tm).astype(ko_ref.dtype)
            vo_ref[:, lo] = v_ref[:, src]
            vo_ref[:, hi] = v_extras

    def src(block0):
        return pl.BlockSpec((tm, PACK), lambda i, hp: (i, block0 + hp))

    out = pl.BlockSpec((tm, 2 * PACK), lambda i, hp: (i, hp))
    shp = jax.ShapeDtypeStruct((S, N_FOX_HEADS * PACK), _CD)
    return _pcall(body, name=name, grid=(S // tm, HEAD_PAIRS),
                  in_specs=[src(Q_BLOCK0), src(K_BLOCK0), src(V_BLOCK0), pl.BlockSpec((tm, PACK), lambda i, hp: (i, 0))],
                  out_specs=[out, out, out], out_shape=[shp, shp, shp],
                  compiler_params=_params("parallel", "parallel"))(qkv, qkv, qkv, c)


def _fox_fwd(qp, kp, vp, *, name):
    S = qp.shape[0]
    nt = S // FOX_T
    nt_dims = (((1,), (1,)), ((), ()))
    tn_dims = (((0,), (0,)), ((), ()))

    def body(i_tab, j_tab, q_ref, k_ref, v_ref, o_ref, l_ref, m_s, acc_s):
        t = pl.program_id(1)
        i, j = i_tab[t], j_tab[t]

        @pl.when(j == 0)
        def _():
            m_s[...] = jnp.full((FOX_HPS, 1, FOX_T), NEG_INF, F32)
            acc_s[...] = jnp.zeros((FOX_HPS, PACK, FOX_T), F32)

        def tile(diagonal):
            for hh in range(FOX_HPS):
                cols = slice(hh * PACK, (hh + 1) * PACK)
                st = lax.dot_general(k_ref[:, cols], q_ref[:, cols], nt_dims, preferred_element_type=F32)
                if diagonal:
                    key = lax.broadcasted_iota(jnp.int32, (FOX_T, FOX_T), 0)
                    qry = lax.broadcasted_iota(jnp.int32, (FOX_T, FOX_T), 1)
                    st = jnp.where(key <= qry, st, NEG_INF)
                m_old = m_s[hh]
                m_new = jnp.maximum(m_old, jnp.max(st, axis=0, keepdims=True))
                pt = jnp.exp(st - m_new)
                acc_s[hh] = jnp.exp(m_old - m_new) * acc_s[hh] + lax.dot_general(
                    v_ref[:, cols], pt.astype(_CD), tn_dims, preferred_element_type=F32)
                m_s[hh] = m_new

        @pl.when(j < i)
        def _():
            tile(False)

        @pl.when(j == i)
        def _():
            tile(True)
            for hh in range(FOX_HPS):
                acc = acc_s[hh]
                den = acc[HEAD_DIM:HEAD_DIM + 1, :]
                cols = slice(hh * HEAD_DIM, (hh + 1) * HEAD_DIM)
                o_ref[:, cols] = (acc[:HEAD_DIM, :] / den).T
                l_ref[:, cols] = jnp.broadcast_to(m_s[hh] + jnp.log(den), (HEAD_DIM, FOX_T)).T

    pairs = [(i, j) for i in range(nt) for j in range(i + 1)]
    i_tab = jnp.asarray([p[0] for p in pairs], jnp.int32)
    j_tab = jnp.asarray([p[1] for p in pairs], jnp.int32)
    qs = pl.BlockSpec((FOX_T, FOX_HPS * PACK), lambda hp, t, it, jt: (it[t], hp))
    ks = pl.BlockSpec((FOX_T, FOX_HPS * PACK), lambda hp, t, it, jt: (jt[t], hp))
    os_ = pl.BlockSpec((FOX_T, FOX_HPS * HEAD_DIM), lambda hp, t, it, jt: (it[t], hp))
    shp = jax.ShapeDtypeStruct((S, FOX_WIDTH), F32)
    grid_spec = pltpu.PrefetchScalarGridSpec(
        num_scalar_prefetch=2, grid=(N_FOX_HEADS // FOX_HPS, len(pairs)), in_specs=[qs, ks, ks], out_specs=[os_, os_],
        scratch_shapes=[pltpu.VMEM((FOX_HPS, 1, FOX_T), F32), pltpu.VMEM((FOX_HPS, PACK, FOX_T), F32)])
    return _pcall(body, name=name, grid_spec=grid_spec, out_shape=[shp, shp],
                  compiler_params=_params("parallel", "arbitrary"))(i_tab, j_tab, qp, kp, vp)


def _fox_pack_bwd(qkv, c, o, lse, do, *, name, tm=1024, after=None):
    S = qkv.shape[0]

    def body(q_ref, c_ref, o_ref, l_ref, do_ref, qo_ref, do_out_ref):
        hp = pl.program_id(1)
        cv = c_ref[...]
        for hh in range(2):
            src = slice(hh * HEAD_DIM, (hh + 1) * HEAD_DIM)
            lo = slice(hh * PACK, hh * PACK + HEAD_DIM)
            hi = slice(hh * PACK + HEAD_DIM, (hh + 1) * PACK)
            shift = _head_column(cv, 2 * hp + hh) - l_ref[:, hh * HEAD_DIM:hh * HEAD_DIM + 1]
            dov = do_ref[:, src]
            dsum = jnp.sum(dov * o_ref[:, src], axis=-1, keepdims=True)
            qo_ref[:, lo] = (q_ref[:, src].astype(F32) * ATTN_SCALE).astype(qo_ref.dtype)
            qo_ref[:, hi] = _extras(_pieces(shift), ONES3, tm).astype(qo_ref.dtype)
            do_out_ref[:, lo] = dov.astype(do_out_ref.dtype)
            do_out_ref[:, hi] = _extras(_pieces(-dsum), ZEROS3, tm).astype(do_out_ref.dtype)

    pair = pl.BlockSpec((tm, PACK), lambda i, hp: (i, hp))
    out = pl.BlockSpec((tm, 2 * PACK), lambda i, hp: (i, hp))
    shp = jax.ShapeDtypeStruct((S, N_FOX_HEADS * PACK), _CD)
    return _pcall(body, after, name=name, grid=(S // tm, HEAD_PAIRS),
                  in_specs=[pl.BlockSpec((tm, PACK), lambda i, hp: (i, Q_BLOCK0 + hp)),
                            pl.BlockSpec((tm, PACK), lambda i, hp: (i, 0)), pair, pair, pair],
                  out_specs=[out, out], out_shape=[shp, shp],
                  compiler_params=_params("parallel", "parallel"))(qkv, c, o, lse, do)


def _fox_bwd(qp, kp, vp, dop, *, name):
    S = qp.shape[0]
    nt = S // FOX_T
    nt_dims = (((1,), (1,)), ((), ()))
    tn_dims = (((0,), (0,)), ((), ()))

    def body(i_tab, j_tab, q_ref, k_ref, v_ref, do_ref, dq_ref, dk_ref, dv_ref, dc_ref, dr_ref,
             dq_s, dk_s, dv_s, dc_s, dr_s):
        t = pl.program_id(1)
        i, j = i_tab[t], j_tab[t]

        @pl.when(t == 0)
        def _():
            dq_s[...] = jnp.zeros((S, FOX_HPS * PACK), F32)
            dr_s[...] = jnp.zeros((FOX_HPS, 1, S), F32)

        @pl.when(i == j)
        def _():
            dk_s[...] = jnp.zeros((FOX_T, FOX_HPS * PACK), F32)
            dv_s[...] = jnp.zeros((FOX_T, FOX_HPS * PACK), F32)
            dc_s[...] = jnp.zeros((FOX_HPS, FOX_T, 1), F32)

        def tile(diagonal):
            rows = pl.ds(pl.multiple_of(i * FOX_T, FOX_T), FOX_T)
            for hh in range(FOX_HPS):
                cols = slice(hh * PACK, (hh + 1) * PACK)
                qv, kv, vv, dov = q_ref[:, cols], k_ref[:, cols], v_ref[:, cols], do_ref[:, cols]
                pt = jnp.exp(lax.dot_general(kv, qv, nt_dims, preferred_element_type=F32))
                if diagonal:
                    key = lax.broadcasted_iota(jnp.int32, (FOX_T, FOX_T), 0)
                    qry = lax.broadcasted_iota(jnp.int32, (FOX_T, FOX_T), 1)
                    pt = jnp.where(key <= qry, pt, 0.0)
                dst = pt * lax.dot_general(vv, dov, nt_dims, preferred_element_type=F32)
                dsb = dst.astype(_CD)
                dc_s[hh] += jnp.sum(dst, axis=1, keepdims=True)
                dr_s[hh, :, rows] += jnp.sum(dst, axis=0, keepdims=True)
                dv_s[:, cols] += jnp.dot(pt.astype(_CD), dov, preferred_element_type=F32)
                dk_s[:, cols] += jnp.dot(dsb, qv, preferred_element_type=F32)
                dq_s[rows, cols] += lax.dot_general(dsb, kv, tn_dims, preferred_element_type=F32)

        @pl.when(i > j)
        def _():
            tile(False)

        @pl.when(i == j)
        def _():
            tile(True)

        @pl.when(i == nt - 1)
        def _():
            for hh in range(FOX_HPS):
                src = slice(hh * PACK, hh * PACK + HEAD_DIM)
                dst_cols = slice(hh * HEAD_DIM, (hh + 1) * HEAD_DIM)
                dk_ref[:, dst_cols] = dk_s[:, src].astype(dk_ref.dtype)
                dv_ref[:, dst_cols] = dv_s[:, src].astype(dv_ref.dtype)
                dc_ref[:, dst_cols] = jnp.broadcast_to(dc_s[hh], (FOX_T, HEAD_DIM))

        @pl.when(t == len(pairs) - 1)
        def _():
            for hh in range(FOX_HPS):
                dq_ref[:, hh * HEAD_DIM:(hh + 1) * HEAD_DIM] = (
                    dq_s[:, hh * PACK:hh * PACK + HEAD_DIM] * ATTN_SCALE).astype(dq_ref.dtype)
            dr_ref[...] = dr_s[...]

    pairs = [(i, j) for j in range(nt) for i in range(j, nt)]
    i_tab = jnp.asarray([p[0] for p in pairs], jnp.int32)
    j_tab = jnp.asarray([p[1] for p in pairs], jnp.int32)
    wide, narrow = FOX_HPS * PACK, FOX_HPS * HEAD_DIM
    qs = pl.BlockSpec((FOX_T, wide), lambda hp, t, it, jt: (it[t], hp))
    ks = pl.BlockSpec((FOX_T, wide), lambda hp, t, it, jt: (jt[t], hp))
    whole = pl.BlockSpec((S, narrow), lambda hp, t, it, jt: (0, hp))
    cs = pl.BlockSpec((FOX_T, narrow), lambda hp, t, it, jt: (jt[t], hp))
    rs = pl.BlockSpec((FOX_HPS, 1, S), lambda hp, t, it, jt: (hp, 0, 0))
    shp = jax.ShapeDtypeStruct((S, FOX_WIDTH), _CD)
    grid_spec = pltpu.PrefetchScalarGridSpec(
        num_scalar_prefetch=2, grid=(N_FOX_HEADS // FOX_HPS, len(pairs)), in_specs=[qs, ks, ks, qs],
        out_specs=[whole, cs, cs, cs, rs],
        scratch_shapes=[pltpu.VMEM((S, wide), F32), pltpu.VMEM((FOX_T, wide), F32),
                        pltpu.VMEM((FOX_T, wide), F32), pltpu.VMEM((FOX_HPS, FOX_T, 1), F32),
                        pltpu.VMEM((FOX_HPS, 1, S), F32)])
    return _pcall(body, name=name, grid_spec=grid_spec,
                  out_shape=[shp, shp, shp, jax.ShapeDtypeStruct((S, FOX_WIDTH), F32),
                             jax.ShapeDtypeStruct((N_FOX_HEADS, 1, S), F32)],
                  compiler_params=_params("parallel", "arbitrary"))(i_tab, j_tab, qp, kp, vp, dop)


def _layer_step(x, tgt, w, p, late_weights=None, grad_sink=None, after=None, first_weights=None):
    S = x.shape[0]
    after_norm, after_proj = after if after is not None else (None, None)
    h = _rms_fwd(x, p["norm_mix_g"], name="rms_mix", after=after_norm)
    if first_weights is not None:
        w = {**w, **first_weights(h)}
    qkv = _mm(h, w["qkv"][:, 3 * DIL_WIDTH:], name="proj_fox", out_dtype=_CD, tn=768, tm=2048, after=after_proj)
    dil_qkv = _proj_dil(h, w["qkv"], name="proj_dil")
    zf = _mm(h, w["f"], name="proj_f")
    gl = _mm(h, w["g"], name="proj_gate", tn=1024, out_dtype=_CD)

    dil_o, dil_l = [], []
    for g in range(N_DIL_GROUPS):
        og, lg = _dil_fwd(dil_qkv[g], g, name=f"dil_fwd{g}")
        dil_o.append(og), dil_l.append(lg)
    o_a = _dil_mix_fwd(dil_o, dil_l, name="dil_mix")

    c = _fox_cumsum(zf, p["b_fgt"], name="fox_cumsum")
    fqp, fkp, fvp = _fox_pack_fwd(qkv, c, name="fox_pack")
    o_b, flse = _fox_fwd(fqp, fkp, fvp, name="fox_fwd")

    if late_weights is not None:
        w = {**w, **late_weights(o_b)}
    y_a = _mm(o_a, w["dil_out"], name="y_a", tn=1024, out_dtype=_CD)
    y_b = _mm(o_b, w["fox_out"], name="y_b", tn=1024, out_dtype=_CD)
    merged, x1, h2 = _gated_mix_out(gl, p["b_gate"], y_a, y_b, w["out"], x, p["norm_ffn_g"], name="mix_out")
    gate, up, act = _ffn_in_act(h2, w["ffn_in"], name="ffn_in")
    loss, dx2, dg_final = _ffn_down_loss(act, w["ffn_down"], x1, p["norm_final_g"], tgt, name="ffn_down_loss")

    gw_ffn_down = _mm(act, dx2, name="gw_ffn_down", ta=True, out_dtype=_CD, tm=1408)
    dgu = _d_swiglu(dx2, w["ffn_down"], gate, up, name="d_swiglu")
    gw_ffn_in = _mm(h2, dgu, name="gw_ffn_in", ta=True, out_dtype=_CD, tn=1408, out_blocks=1408, b_halves=True)
    sink = grad_sink if grad_sink is not None else (lambda group, grads: None)
    tok = sink("ffn", dict(ffn_in=gw_ffn_in, ffn_down=gw_ffn_down))
    dx1, dg_ffn = _mm(dgu, w["ffn_in"], name="d_h2", tb=True, tk=1408, b_blocks=True, tm=1024, a_halves=True,
                      rms_bwd=(x1, p["norm_ffn_g"], dx2), after=tok)

    gw_out = _mm(merged, dx1, name="gw_out", ta=True, out_dtype=_CD)
    dy_a, dy_b, dgl, db_gate = _gate_bwd(dx1, w["out"], gl, p["b_gate"], y_a, y_b, name="gate_bwd")
    do_a = _mm(dy_a, w["dil_out"], name="d_o_a", tb=True)
    gw_dil_out = _mm(o_a, dy_a, name="gw_dil_out", ta=True, out_dtype=_CD, tn=1024)
    do_b = _mm(dy_b, w["fox_out"], name="d_o_b", tb=True)
    gw_fox_out = _mm(o_b, dy_b, name="gw_fox_out", ta=True, out_dtype=_CD, tn=1024)
    tok = sink("mix", dict(dil_out=gw_dil_out, fox_out=gw_fox_out, out=gw_out))

    bqp, bdop = _fox_pack_bwd(qkv, c, o_b, flse, do_b, name="fox_pack_bwd", after=tok)
    dqp, dkp, dvp, dck, dcq = _fox_bwd(bqp, fkp, fvp, bdop, name="fox_bwd")
    dc = dcq[:, 0, :].T - dck.reshape(S, N_FOX_HEADS, HEAD_DIM)[:, :, 0]
    dc = jnp.pad(dc, ((0, 0), (0, F_PAD - N_FOX_HEADS)))
    dzf, db_fgt = _fox_cumsum_bwd(dc, zf, p["b_fgt"], name="fox_cumsum_bwd")

    douts = _dil_mix_bwd(do_a, dil_o, dil_l, name="dil_mix_bwd", after=tok)
    dqs, dks, dvs = [], [], []
    for g in range(N_DIL_GROUPS):
        dq, dk, dv = _dil_bwd(dil_qkv[g], douts[3 + g], douts[g], g, name=f"dil_bwd{g}")
        for parts, t in ((dqs, dq), (dks, dk), (dvs, dv)):
            parts.extend([t[0].astype(_CD), t[1].astype(_CD)])
    dqkv = jnp.concatenate(dqs + dks + dvs + [dqp, dkp, dvp], axis=1)

    gw_qkv = _mm(h, dqkv, name="gw_qkv", ta=True, out_dtype=_CD, tn=768)
    gw_g = _mm(h, dgl, name="gw_gate", ta=True, out_dtype=_CD)
    gw_f = _mm(h, dzf, name="gw_f", ta=True, out_dtype=_CD)
    tok = sink("in", dict(qkv=gw_qkv, f=gw_f, g=gw_g))
    dx, dg_mix = _mm(dqkv, w["qkv"], name="d_h", tb=True, tk=QKV_COLS, tm=256, more=((dgl, w["g"]), (dzf, w["f"])),
                     rms_bwd=(x, p["norm_mix_g"], dx1), after=tok)

    gw = dict(qkv=gw_qkv, f=gw_f, g=gw_g, dil_out=gw_dil_out, fox_out=gw_fox_out, out=gw_out, ffn_in=gw_ffn_in,
              ffn_down=gw_ffn_down)
    small = dict(norm_mix_g=dg_mix, b_fgt=db_fgt, b_gate=db_gate, norm_ffn_g=dg_ffn, norm_final_g=dg_final)
    return loss, dx, gw, small


def _position():
    return lax.axis_index("x"), lax.axis_index("y"), lax.axis_index("c")


def _other_chips(x, y):
    return [(1 - x, y), (x, 1 - y), (1 - x, 1 - y)]


ROW_TILE = 16


def _row_chunks(rows, want=4):
    n = want
    while n > 1 and rows % (n * ROW_TILE):
        n //= 2
    return n


SEM_SPEC = pl.BlockSpec(memory_space=pltpu.SEMAPHORE)
ANY_SPEC = pl.BlockSpec(memory_space=pl.ANY)
DATAFLOW = pltpu.SideEffectType.DATAFLOW_SIDE_EFFECTING


def _in_hbm(a):
    return pltpu.with_memory_space_constraint(a, pltpu.HBM)


def _split_copy_start(srcs, land_shapes, copies, after, *, name):
    n, m = len(srcs), len(land_shapes)

    def body(*refs):
        src_refs, land_refs = refs[:n], refs[n:n + m]
        send_sems, recv_sems = refs[n + m + 1], refs[n + m + 2]
        token = refs[-1]
        x, y, c = _position()
        for k, (src, dst, peer) in enumerate(copies(x, y, c, src_refs, land_refs)):
            pltpu.make_async_remote_copy(src_ref=src, dst_ref=dst, send_sem=send_sems.at[k], recv_sem=recv_sems.at[k],
                                         device_id=peer, device_id_type=MESH).start()
        token[...] = jnp.zeros_like(token)

    lands = [lax.empty(s.shape, s.dtype) for s in land_shapes]
    count = len(copies(0, 0, 0, srcs, lands))
    out = _pcall(
        body, name=name,
        out_shape=(pltpu.SemaphoreType.DMA((count,)), pltpu.SemaphoreType.DMA((count,)),
                   *[pltpu.HBM(s.shape, s.dtype) for s in srcs], *[pltpu.HBM(s.shape, s.dtype) for s in land_shapes],
                   jax.ShapeDtypeStruct((8, 128), F32)),
        in_specs=[HBM_SPEC] * (n + m) + [ANY_SPEC],
        out_specs=(SEM_SPEC, SEM_SPEC, *[HBM_SPEC] * (n + m), pl.BlockSpec(memory_space=pltpu.VMEM)),
        input_output_aliases={k: 2 + k for k in range(n + m)},
        compiler_params=pltpu.CompilerParams(has_side_effects=DATAFLOW),
    )(*[_in_hbm(s) for s in srcs], *[_in_hbm(l) for l in lands], after)
    return out[0], out[1], list(out[2:2 + n]), list(out[2 + n:2 + n + m]), out[-1]


def _split_copy_wait(send_sems, recv_sems, srcs, lands, copies, after, *, name):
    n, m = len(srcs), len(lands)

    def body(*refs):
        src_refs, land_refs = refs[:n], refs[n:n + m]
        send, recv = refs[n + m], refs[n + m + 1]
        x, y, c = _position()
        for k, (src, dst, peer) in enumerate(copies(x, y, c, src_refs, land_refs)):
            cp = pltpu.make_async_remote_copy(src_ref=src, dst_ref=dst, send_sem=send.at[k], recv_sem=recv.at[k],
                                              device_id=peer, device_id_type=MESH)
            cp.wait_send()
            cp.wait_recv()

    afters = list(after) if isinstance(after, (list, tuple)) else [after]
    out = _pcall(
        body, name=name,
        out_shape=tuple(pltpu.HBM(s.shape, s.dtype) for s in list(srcs) + list(lands)),
        in_specs=[HBM_SPEC] * (n + m) + [SEM_SPEC, SEM_SPEC] + [ANY_SPEC] * len(afters),
        out_specs=tuple([HBM_SPEC] * (n + m)),
        input_output_aliases={k: k for k in range(n + m)},
        compiler_params=pltpu.CompilerParams(has_side_effects=DATAFLOW),
    )(*srcs, *lands, send_sems, recv_sems, *afters)
    return list(out[:n]), list(out[n:])


def _gather_copies(x, y, c, shard_refs, land_refs):
    out = []
    for s, l in zip(shard_refs, land_refs):
        half = s.shape[0] // 2
        nq = _row_chunks(half)
        for cx, cy in _other_chips(x, y):
            for q in range(nq):
                rows = pl.ds(c * half + q * (half // nq), half // nq)
                out.append((s.at[rows, :], l.at[2 * x + y, rows, :], (cx, cy, c)))
    return out


def _gather_whole_copies(x, y, c, shard_refs, land_refs):
    out = []
    for s, l in zip(shard_refs, land_refs):
        nq = _row_chunks(s.shape[0])
        for cx, cy in _other_chips(x, y):
            for q in range(nq):
                rows = pl.ds(q * (s.shape[0] // nq), s.shape[0] // nq)
                out.append((s.at[rows, :], l.at[2 * x + y, rows, :], (cx, cy, c)))
    return out


def _scatter_all_copies(x, y, c, block_refs, land_refs):
    out = []
    for g, l in zip(block_refs, land_refs):
        half = g.shape[1] // 2
        nq = _row_chunks(half)
        size = half // nq
        for q in range(nq):
            rows = pl.ds((1 - c) * half + q * size, size)
            out.append((g.at[2 * x + y, rows, :], l.at[0, pl.ds(q * size, size), :], (x, y, 1 - c)))
        for r, (cx, cy) in enumerate(_other_chips(x, y)):
            for j in range(2):
                h = c if j == 0 else 1 - c
                for q in range(nq):
                    rows = pl.ds(h * half + q * size, size)
                    out.append((g.at[2 * cx + cy, rows, :], l.at[1 + 2 * r + j, pl.ds(q * size, size), :], (cx, cy, h)))
    return out


def _forward_halves(lands, *, name):
    n = len(lands)

    def body(*refs):
        ins = refs[:n]
        send_sems, recv_sems = refs[2 * n:]
        x, y, c = _position()
        copies = []
        for w in range(n):
            half = ins[w].shape[1] // 2
            for r, (cx, cy) in enumerate(_other_chips(x, y)):
                blk = ins[w].at[2 * cx + cy, pl.ds(c * half, half), :]
                cp = pltpu.make_async_remote_copy(src_ref=blk, dst_ref=blk, send_sem=send_sems.at[w, r],
                                                  recv_sem=recv_sems.at[w, r], device_id=(x, y, 1 - c),
                                                  device_id_type=MESH)
                cp.start()
                copies.append(cp)
        for w in range(n):
            half = ins[w].shape[1] // 2
            for r, (cx, cy) in enumerate(_other_chips(x, y)):
                blk = ins[w].at[2 * cx + cy, pl.ds((1 - c) * half, half), :]
                pltpu.make_async_remote_copy(src_ref=blk, dst_ref=blk, send_sem=send_sems.at[w, r],
                                             recv_sem=recv_sems.at[w, r], device_id=(x, y, 1 - c),
                                             device_id_type=MESH).wait_recv()
        for cp in copies:
            cp.wait_send()

    return _pcall(
        body, name=name, in_specs=[HBM_SPEC] * n, out_specs=[HBM_SPEC] * n,
        out_shape=[jax.ShapeDtypeStruct(l.shape, l.dtype) for l in lands],
        input_output_aliases={k: k for k in range(n)},
        scratch_shapes=[pltpu.SemaphoreType.DMA((n, 3)), pltpu.SemaphoreType.DMA((n, 3))],
    )(*lands)


def _share_halves(halves, *, name):
    n = len(halves)

    def body(*refs):
        ins, outs = refs[:n], refs[n:2 * n]
        send_sems, recv_sems = refs[2 * n:]
        x, y, c = _position()
        copies = []
        for w in range(n):
            cp = pltpu.make_async_remote_copy(src_ref=ins[w], dst_ref=outs[w], send_sem=send_sems.at[w],
                                              recv_sem=recv_sems.at[w], device_id=(x, y, 1 - c), device_id_type=MESH)
            cp.start()
            copies.append(cp)
        for cp in copies:
            cp.wait()

    return _pcall(
        body, name=name, in_specs=[HBM_SPEC] * n, out_specs=[HBM_SPEC] * n,
        out_shape=[jax.ShapeDtypeStruct(h.shape, h.dtype) for h in halves],
        scratch_shapes=[pltpu.SemaphoreType.DMA((n,)), pltpu.SemaphoreType.DMA((n,))],
    )(*halves)


def _sum_small(part):
    rows, width = part.shape

    def body(x_ref, out_ref, all_ref, send_sems, recv_sems):
        x, y, c = _position()
        me, sibling = (x, y, c), (x, y, 1 - c)
        chips = _other_chips(x, y)

        def block(px, py, pc):
            return all_ref.at[pl.ds((4 * px + 2 * py + pc) * rows, rows), :]

        def copy(k, blk, to, src=None):
            return pltpu.make_async_remote_copy(
                src_ref=block(*blk) if src is None else src, dst_ref=block(*blk), send_sem=send_sems.at[k],
                recv_sem=recv_sems.at[k], device_id=to, device_id_type=MESH)

        all_ref[pl.ds((4 * x + 2 * y + c) * rows, rows), :] = x_ref[...]
        first = [copy(0, me, sibling, src=x_ref)]
        first += [copy(1 + j, me, (*chip, c), src=x_ref) for j, chip in enumerate(chips)]
        for cp in first:
            cp.start()
        passed = [copy(4 + j, (*chip, c), sibling) for j, chip in enumerate(chips)]
        for j, chip in enumerate(chips):
            copy(1 + j, (*chip, c), me).wait_recv()
            passed[j].start()
        copy(0, sibling, me).wait_recv()
        for j, chip in enumerate(chips):
            copy(4 + j, (*chip, 1 - c), me).wait_recv()
        for cp in first + passed:
            cp.wait_send()
        total = all_ref[0:rows, :]
        for d in range(1, 8):
            total = total + all_ref[d * rows:(d + 1) * rows, :]
        out_ref[...] = total

    vm = pl.BlockSpec(memory_space=pltpu.VMEM)
    return _pcall(
        body, name="sum_small", in_specs=[vm], out_specs=vm, out_shape=jax.ShapeDtypeStruct((rows, width), F32),
        scratch_shapes=[pltpu.VMEM((8 * rows, width), F32), pltpu.SemaphoreType.DMA((7,)), pltpu.SemaphoreType.DMA((7,))],
    )(part)


def _row_tile(R, C, itemsize=4, budget=1 << 20):
    for t in (512, 256, 128, 64, 32, 16, 8):
        if R % t == 0 and t * C * itemsize <= budget:
            return t
    return R


def _add_all(g, recv, where, *, name):
    _, R, C = g.shape
    half = R // 2
    t = _row_tile(half, C)
    nb = half // t

    def body(w_ref, g_ref, r_ref, o_ref):
        total = g_ref[0].astype(F32)
        for k in range(7):
            total = total + r_ref[k].astype(F32)
        o_ref[...] = total

    grid_spec = pltpu.PrefetchScalarGridSpec(
        num_scalar_prefetch=1, grid=(nb,),
        in_specs=[pl.BlockSpec((1, t, C), lambda i, wr: (wr[0], wr[1] * nb + i, 0)),
                  pl.BlockSpec((7, t, C), lambda i, wr: (0, i, 0))],
        out_specs=pl.BlockSpec((t, C), lambda i, wr: (i, 0)))
    return _pcall(body, name=name, grid_spec=grid_spec, out_shape=jax.ShapeDtypeStruct((half, C), F32),
                  compiler_params=_params("parallel"))(where, g, recv)


def _adamw(w, g, m, v, *, name):
    R, C = w.shape
    t = _row_tile(R, C)
    c1 = 1.0 - ADAM_B1 ** ADAM_STEP
    c2 = 1.0 - ADAM_B2 ** ADAM_STEP

    def body(w_ref, g_ref, m_ref, v_ref, d_ref, nm_ref, nv_ref):
        gv = g_ref[...]
        mn = ADAM_B1 * m_ref[...] + (1.0 - ADAM_B1) * gv
        vn = ADAM_B2 * v_ref[...] + (1.0 - ADAM_B2) * (gv * gv)
        d_ref[...] = -ADAM_LR * ((mn / c1) / (jnp.sqrt(vn / c2) + ADAM_EPS) + ADAM_WD * w_ref[...])
        nm_ref[...] = mn
        nv_ref[...] = vn

    blk = pl.BlockSpec((t, C), lambda i: (i, 0))
    shp = jax.ShapeDtypeStruct((R, C), F32)
    return _pcall(body, name=name, grid=(R // t,), in_specs=[blk] * 4, out_specs=[blk] * 3, out_shape=[shp] * 3,
                  compiler_params=_params("parallel"))(w, g, m, v)


def _adamw_halves(w, mine, theirs, m, v, core, *, name):
    R, C = w.shape
    half = R // 2
    t = _row_tile(half, C)
    nbh = half // t
    c1 = 1.0 - ADAM_B1 ** ADAM_STEP
    c2 = 1.0 - ADAM_B2 ** ADAM_STEP

    def body(core_ref, w_ref, a_ref, b_ref, m_ref, v_ref, g_ref, d_ref, nm_ref, nv_ref):
        gv = jnp.where(pl.program_id(0) // nbh == core_ref[0], a_ref[...], b_ref[...])
        mn = ADAM_B1 * m_ref[...] + (1.0 - ADAM_B1) * gv
        vn = ADAM_B2 * v_ref[...] + (1.0 - ADAM_B2) * (gv * gv)
        g_ref[...] = gv
        d_ref[...] = -ADAM_LR * ((mn / c1) / (jnp.sqrt(vn / c2) + ADAM_EPS) + ADAM_WD * w_ref[...])
        nm_ref[...] = mn
        nv_ref[...] = vn

    blk = pl.BlockSpec((t, C), lambda i, cr: (i, 0))
    hblk = pl.BlockSpec((t, C), lambda i, cr: (i % nbh, 0))
    shp = jax.ShapeDtypeStruct((R, C), F32)
    grid_spec = pltpu.PrefetchScalarGridSpec(num_scalar_prefetch=1, grid=(2 * nbh,),
                                             in_specs=[blk, hblk, hblk, blk, blk], out_specs=[blk] * 4)
    return _pcall(body, name=name, grid_spec=grid_spec, out_shape=[shp] * 4,
                  compiler_params=_params("parallel"))(core, w, mine, theirs, m, v)


BIG = ("w_in", "w_dil_out", "w_fox_out", "w_out", "w_ffn_in", "w_ffn_down")
SMALL = ("norm_mix_g", "b_fgt", "b_gate", "norm_ffn_g", "norm_final_g")
ORDER = ("norm_mix_g", "w_in", "b_fgt", "b_gate", "w_dil_out", "w_fox_out", "w_out", "norm_ffn_g", "w_ffn_in",
         "w_ffn_down", "norm_final_g")
SMALL_ROWS = {"norm_mix_g": (0, 1), "b_gate": (1, 3), "norm_ffn_g": (3, 4), "norm_final_g": (4, 5), "b_fgt": (5, 6)}


def _columns_to_blocks(full, ncol):
    K = full.shape[0]
    return full.reshape(K, 4, ncol).transpose(1, 0, 2)


def _blocks_to_columns(blocks):
    n, K, ncol = blocks.shape
    return blocks.transpose(1, 0, 2).reshape(K, n * ncol)


def kernel(x, norm_mix_g, w_in, b_fgt, b_gate, w_dil_out, w_fox_out, w_out, norm_ffn_g, w_ffn_in, w_ffn_down, norm_final_g, loss_target, m_norm_mix_g, m_w_in, m_b_fgt, m_b_gate, m_w_dil_out, m_w_fox_out, m_w_out, m_norm_ffn_g, m_w_ffn_in, m_w_ffn_down, m_norm_final_g, v_norm_mix_g, v_w_in, v_b_fgt, v_b_gate, v_w_dil_out, v_w_fox_out, v_w_out, v_norm_ffn_g, v_w_ffn_in, v_w_ffn_down, v_norm_final_g):
    weights = dict(norm_mix_g=norm_mix_g, w_in=w_in, b_fgt=b_fgt, b_gate=b_gate, w_dil_out=w_dil_out,
                   w_fox_out=w_fox_out, w_out=w_out, norm_ffn_g=norm_ffn_g, w_ffn_in=w_ffn_in, w_ffn_down=w_ffn_down,
                   norm_final_g=norm_final_g)
    m_in = dict(norm_mix_g=m_norm_mix_g, w_in=m_w_in, b_fgt=m_b_fgt, b_gate=m_b_gate, w_dil_out=m_w_dil_out,
                w_fox_out=m_w_fox_out, w_out=m_w_out, norm_ffn_g=m_norm_ffn_g, w_ffn_in=m_w_ffn_in,
                w_ffn_down=m_w_ffn_down, norm_final_g=m_norm_final_g)
    v_in = dict(norm_mix_g=v_norm_mix_g, w_in=v_w_in, b_fgt=v_b_fgt, b_gate=v_b_gate, w_dil_out=v_w_dil_out,
                w_fox_out=v_w_fox_out, w_out=v_w_out, norm_ffn_g=v_norm_ffn_g, w_ffn_in=v_w_ffn_in,
                w_ffn_down=v_w_ffn_down, norm_final_g=v_norm_final_g)
    c = lax.axis_index("c")
    chip = 2 * lax.axis_index("x") + lax.axis_index("y")

    shards = {n: weights[n][0].astype(_CD) for n in BIG}
    in_shape = jax.ShapeDtypeStruct((4,) + shards["w_in"].shape, _CD)
    send_i, recv_i, in_src, in_land, token_in = _split_copy_start(
        [shards["w_in"]], [in_shape], _gather_copies, norm_mix_g, name="gather_in_start")
    late = BIG[1:]
    send_g, recv_g, late_src, late_land, token = _split_copy_start(
        [shards[n] for n in late], [jax.ShapeDtypeStruct((4,) + shards[n].shape, _CD) for n in late],
        _gather_whole_copies, token_in, name="gather_late_start")
    adam_in = [t[0] + token_in[0, 0] for t in (w_in, m_w_in, v_w_in)]
    p = dict(norm_mix_g=norm_mix_g, b_fgt=jnp.pad(b_fgt, ((0, 0), (0, F_PAD - N_FOX_HEADS))), b_gate=b_gate,
             norm_ffn_g=norm_ffn_g, norm_final_g=norm_final_g.reshape(1, D_MODEL))

    def first_weights(after):
        own, lands = _split_copy_wait(send_i, recv_i, in_src, in_land, _gather_copies, [after] + adam_in,
                                      name="gather_in_wait")
        (g_in,) = _forward_halves(lands, name="gather_in_forward")
        full_in = _blocks_to_columns(lax.dynamic_update_index_in_dim(g_in, own[0], chip, 0))
        o3 = QKV_COLS
        o4 = o3 + N_FOX_HEADS
        return dict(qkv=full_in[:, :o3], f=jnp.pad(full_in[:, o3:o4], ((0, 0), (0, F_PAD - N_FOX_HEADS))),
                    g=full_in[:, o4:])

    def late_weights(after):
        own, lands = _split_copy_wait(send_g, recv_g, late_src, late_land, _gather_whole_copies, after,
                                      name="gather_late_wait")
        g_dil, g_fox, g_out, g_ffn_in, g_ffn_down = [
            lax.dynamic_update_index_in_dim(l, s, chip, 0) for l, s in zip(lands, own)]
        return dict(dil_out=_blocks_to_columns(g_dil), fox_out=_blocks_to_columns(g_fox),
                    out=g_out.reshape(D_MODEL, D_MODEL), ffn_in=g_ffn_in,
                    ffn_down=g_ffn_down.reshape(D_FF, D_MODEL))

    def to_blocks(n, full):
        shape = weights[n].shape
        if full.ndim == 3:
            return full
        if n in ("w_out", "w_ffn_down"):
            return full.reshape(4, shape[1], shape[2])
        return _columns_to_blocks(full, shape[2])

    in_flight = {}

    def grad_sink(group, gw):
        if group == "in":
            named = {"w_in": jnp.concatenate([gw["qkv"], gw["f"][:, :N_FOX_HEADS], gw["g"]], axis=1)}
        else:
            named = {"w_" + k: v for k, v in gw.items()}
        srcs = [to_blocks(n, named[n]) for n in named]
        lands = [jax.ShapeDtypeStruct((7, s.shape[1] // 2, s.shape[2]), s.dtype) for s in srcs]
        started = _split_copy_start(srcs, lands, _scatter_all_copies, next(iter(gw.values())),
                                    name=f"scatter_{group}_start")
        in_flight[group] = (list(named), started)
        return started[-1]

    loss_part, grad_x, gw, small = _layer_step(x[0], loss_target[0], {}, p, late_weights, grad_sink,
                                               (token_in, token), first_weights)

    where = jnp.stack([chip, c]).astype(jnp.int32)

    def summed_halves(groups, after, name):
        halves = {}
        for group in groups:
            names, (send_s, recv_s, srcs, lands, _) = in_flight[group]
            srcs, recv = _split_copy_wait(send_s, recv_s, srcs, lands, _scatter_all_copies, after,
                                          name=f"scatter_{group}_wait")
            halves.update({n: _add_all(s, r, where, name=f"add_all_{n}") for n, s, r in zip(names, srcs, recv)})
        return {n: (h, o) for (n, h), o in zip(halves.items(), _share_halves(list(halves.values()), name=name))}

    grad_halves = summed_halves(("ffn", "mix"), grad_x, "share_halves")

    packed = jnp.concatenate([
        small["norm_mix_g"], small["b_gate"].reshape(2, D_MODEL), small["norm_ffn_g"], small["norm_final_g"],
        jnp.pad(small["b_fgt"], ((0, 0), (0, D_MODEL - F_PAD))), jnp.pad(loss_part, ((0, 0), (0, D_MODEL - 1))),
        jnp.zeros((1, D_MODEL), F32)], axis=0)
    summed = _sum_small(packed)
    loss = summed[6, 0]

    out_g, out_d, out_m, out_v = {}, {}, {}, {}
    core = jnp.reshape(c, (1,)).astype(jnp.int32)
    for n in SMALL:
        lo, hi = SMALL_ROWS[n]
        shape = weights[n].shape
        g2 = summed[lo:hi].reshape(1, -1)[:, :weights[n].size]
        d2, m2, v2 = _adamw(weights[n].reshape(g2.shape), g2, m_in[n].reshape(g2.shape), v_in[n].reshape(g2.shape),
                            name=f"adamw_{n}")
        out_g[n], out_d[n], out_m[n], out_v[n] = [t.reshape(shape) for t in (g2, d2, m2, v2)]
    for n in [b for b in BIG if b != "w_in"] + ["w_in"]:
        if n == "w_in":
            done = [summed] + [out_d[b] for b in BIG if b != "w_in"]
            grad_halves.update(summed_halves(("in",), [grad_x] + done, "share_halves_in"))
        shape = weights[n].shape
        wmv = adam_in if n == "w_in" else [t[0] for t in (weights[n], m_in[n], v_in[n])]
        mine, theirs = grad_halves[n]
        outs = _adamw_halves(wmv[0], mine, theirs, wmv[1], wmv[2], core, name=f"adamw_{n}")
        out_g[n], out_d[n], out_m[n], out_v[n] = [t.reshape(shape) for t in outs]
    return (loss, grad_x[None], *[out_g[n] for n in ORDER], *[out_d[n] for n in ORDER],
            *[out_m[n] for n in ORDER], *[out_v[n] for n in ORDER])
```

```python
import numpy as np
import jax
import jax.numpy as jnp
from jax import lax
from jax.experimental import pallas as pl
from jax.experimental.pallas import tpu as pltpu

F32 = jnp.float32
_CD = jnp.bfloat16

D_MODEL = 1024
HEAD_DIM = 64
DIL_PAIRS = ((128, 1), (512, 4), (2048, 16))
N_DIL_GROUPS = 3
DIL_HEADS = 4
DIL_W = 128
DIL_OUT = DIL_HEADS * HEAD_DIM
DIL_WIDTH = N_DIL_GROUPS * DIL_OUT
N_FOX_HEADS = 8
FOX_WIDTH = N_FOX_HEADS * HEAD_DIM
D_FF = 2816
QKV_COLS = 3 * DIL_WIDTH + 3 * FOX_WIDTH
F_PAD = 128
RMS_EPS = 1e-6
NEG_INF = -1e30
ATTN_SCALE = HEAD_DIM ** -0.5
ADAM_LR, ADAM_B1, ADAM_B2, ADAM_EPS, ADAM_WD, ADAM_STEP = 0.001, 0.9, 0.999, 1e-08, 0.01, 10

VMEM_LIMIT = 48 * 1024 * 1024
VMEM_LIMIT_RESIDENT = 56 * 1024 * 1024
LANES = 128
MESH = pl.DeviceIdType.MESH
HBM_SPEC = pl.BlockSpec(memory_space=pltpu.HBM)


def _pcall(body, after=None, **kw):
    if after is None:
        return pl.pallas_call(body, **kw)
    n_in = len(kw["in_specs"])
    kw["in_specs"] = list(kw["in_specs"]) + [pl.BlockSpec(memory_space=pl.ANY)]

    def tied(*refs):
        return body(*refs[:n_in], *refs[n_in + 1:])

    call = pl.pallas_call(tied, **kw)
    return lambda *args: call(*args, after)


def _params(*sem):
    return pltpu.CompilerParams(dimension_semantics=sem, vmem_limit_bytes=VMEM_LIMIT)


def _pick(dim, pref):
    t = (min(pref, dim) // 128) * 128
    while t >= 128:
        if dim % t == 0:
            return t
        t -= 128
    return dim


def _mm(a, b, *, name, ta=False, tb=False, out_dtype=F32, add=None, tm=1024, tn=512, tk=2048, after=None,
        b_blocks=False, out_blocks=None, a_halves=False, b_halves=False, rms_bwd=None, more=()):
    if a_halves:
        M, K = a.shape[1], 2 * a.shape[2]
    elif ta:
        K, M = a.shape
    else:
        M, K = a.shape
    if b_halves:
        b_rows, b_cols = b.shape[1], 2 * b.shape[2]
    else:
        b_rows, b_cols = (b.shape[1], b.shape[0] * b.shape[2]) if b_blocks else b.shape
    if tb:
        N, K2 = b_rows, b_cols
    else:
        K2, N = b_rows, b_cols
    assert K == K2, (a.shape, b.shape)
    shard = b.shape[2] if b_blocks else None
    tm = _pick(M, tm)
    tn = _pick(shard if (b_blocks and not tb) else (out_blocks or N), tn)
    tk = _pick(shard if (b_blocks and tb) else K, tk)
    nk = K // tk
    dn = (((0 if ta else 1,), (1 if tb else 0,)), ((), ()))
    has_add = add is not None
    assert not (has_add and out_blocks)
    has_norm = rms_bwd is not None
    if has_norm:
        tn = N
        assert not out_blocks and out_dtype == F32
    assert not more or (nk == 1 and tb and not ta)

    def body(*refs):
        a_ref, b_ref = refs[0], refs[1]
        rest = list(refs[2:])
        more_refs = [(rest.pop(0), rest.pop(0)) for _ in more]
        add_ref = rest.pop(0) if has_add else None
        x_ref, g_ref, dres_ref = (rest.pop(0), rest.pop(0), rest.pop(0)) if has_norm else (None, None, None)
        o_ref = rest.pop(0)
        dg_ref = rest.pop(0) if has_norm else None
        bv = b_ref[0] if b_blocks else b_ref[...]
        p = lax.dot_general(a_ref[...].astype(_CD), bv.astype(_CD), dn, preferred_element_type=F32)
        for a2_ref, b2_ref in more_refs:
            p += lax.dot_general(a2_ref[...].astype(_CD), b2_ref[...].astype(_CD), dn, preferred_element_type=F32)

        def finish(r):
            if has_add:
                r = r + add_ref[...]
            if has_norm:
                xv = x_ref[...]
                rs = lax.rsqrt(jnp.mean(xv * xv, axis=-1, keepdims=True) + RMS_EPS)
                xh = xv * rs
                dxh = r * g_ref[...]
                o_ref[...] = dres_ref[...] + rs * (dxh - xh * jnp.mean(dxh * xh, axis=-1, keepdims=True))
                part = jnp.sum(r * xh, axis=0, keepdims=True)
                first = pl.program_id(0) == 0

                @pl.when(first)
                def _():
                    dg_ref[...] = part

                @pl.when(jnp.logical_not(first))
                def _():
                    dg_ref[...] += part
            elif out_blocks:
                o_ref[0] = r.astype(out_dtype)
            else:
                o_ref[...] = r.astype(out_dtype)

        if nk == 1:
            finish(p)
        else:
            acc_ref = rest.pop(0)
            k = pl.program_id(2)

            @pl.when(k == 0)
            def _():
                acc_ref[...] = p

            @pl.when(k > 0)
            def _():
                acc_ref[...] += p

            @pl.when(k == nk - 1)
            def _():
                finish(acc_ref[...])

    if a_halves:
        ka = (K // 2) // tk
        a_spec = pl.BlockSpec((None, tm, tk), lambda i, j, k: (k // ka, i, k % ka))
    else:
        a_spec = pl.BlockSpec((tk, tm), lambda i, j, k: (k, i)) if ta else pl.BlockSpec((tm, tk), lambda i, j, k: (i, k))
    if b_halves:
        nb_ = (N // 2) // tn
        b_spec = pl.BlockSpec((None, tk, tn), lambda i, j, k: (j // nb_, k, j % nb_))
    elif b_blocks and tb:
        per = shard // tk
        b_spec = pl.BlockSpec((1, tn, tk), lambda i, j, k: (k // per, j, k % per))
    elif b_blocks:
        per = shard // tn
        b_spec = pl.BlockSpec((1, tk, tn), lambda i, j, k: (j // per, k, j % per))
    else:
        b_spec = pl.BlockSpec((tn, tk), lambda i, j, k: (j, k)) if tb else pl.BlockSpec((tk, tn), lambda i, j, k: (k, j))
    if out_blocks:
        oper = out_blocks // tn
        o_spec = pl.BlockSpec((1, tm, tn), lambda i, j, k: (j // oper, i, j % oper))
        out_shape = jax.ShapeDtypeStruct((N // out_blocks, M, out_blocks), out_dtype)
    else:
        o_spec = pl.BlockSpec((tm, tn), lambda i, j, k: (i, j))
        out_shape = jax.ShapeDtypeStruct((M, N), out_dtype)
    in_specs, args = [a_spec, b_spec], (a, b)
    for a2, b2 in more:
        assert a2.shape[0] == M and b2.shape == (N, a2.shape[1]), (a2.shape, b2.shape)
        in_specs += [pl.BlockSpec((tm, a2.shape[1]), lambda i, j, k: (i, 0)),
                     pl.BlockSpec((tn, a2.shape[1]), lambda i, j, k: (j, 0))]
        args += (a2, b2)
    if has_add:
        in_specs, args = in_specs + [o_spec], args + (add,)
    out_specs, semantics = o_spec, ("parallel", "parallel", "arbitrary")
    if has_norm:
        vec = pl.BlockSpec((1, N), lambda i, j, k: (0, 0))
        in_specs += [o_spec, vec, o_spec]
        args += tuple(rms_bwd)
        out_specs, out_shape = [o_spec, vec], [out_shape, jax.ShapeDtypeStruct((1, N), F32)]
        semantics = ("arbitrary", "arbitrary", "arbitrary")
    return _pcall(
        body, after, name=name, grid=(M // tm, N // tn, nk), in_specs=in_specs, out_specs=out_specs,
        out_shape=out_shape,
        scratch_shapes=[pltpu.VMEM((tm, tn), F32)] if nk > 1 else [],
        compiler_params=_params(*semantics),
    )(*args)


def _rms_fwd(x, g, *, name, tm=512, after=None):
    S, D = x.shape

    def body(x_ref, g_ref, h_ref):
        xv = x_ref[...]
        r = lax.rsqrt(jnp.mean(xv * xv, axis=-1, keepdims=True) + RMS_EPS)
        h_ref[...] = ((xv * r) * g_ref[...]).astype(h_ref.dtype)

    row = pl.BlockSpec((tm, D), lambda i: (i, 0))
    return _pcall(body, after, name=name, grid=(S // tm,), in_specs=[row, pl.BlockSpec((1, D), lambda i: (0, 0))],
                  out_specs=row, out_shape=jax.ShapeDtypeStruct((S, D), _CD), compiler_params=_params("parallel"))(x, g)


def _ffn_down_loss(act, w_down, x1, g, tgt, *, name, tm=512):
    S, D = x1.shape
    F = act.shape[1]

    def body(a_ref, b_ref, x_ref, g_ref, t_ref, loss_ref, dx_ref, dg_ref):
        xv = x_ref[...] + jnp.dot(a_ref[...].astype(_CD), b_ref[...].astype(_CD), preferred_element_type=F32)
        gv = g_ref[...]
        r = lax.rsqrt(jnp.mean(xv * xv, axis=-1, keepdims=True) + RMS_EPS)
        xh = xv * r
        err = xh * gv - t_ref[...]
        lpart = 0.5 * jnp.sum(jnp.mean(err * err, axis=-1, keepdims=True), axis=0, keepdims=True)
        dy = err * (1.0 / D)
        dxh = dy * gv
        dx_ref[...] = r * (dxh - xh * jnp.mean(dxh * xh, axis=-1, keepdims=True))
        gpart = jnp.sum(dy * xh, axis=0, keepdims=True)

        @pl.when(pl.program_id(0) == 0)
        def _():
            loss_ref[...] = lpart
            dg_ref[...] = gpart

        @pl.when(pl.program_id(0) > 0)
        def _():
            loss_ref[...] += lpart
            dg_ref[...] += gpart

    row = pl.BlockSpec((tm, D), lambda i: (i, 0))
    vec = pl.BlockSpec((1, D), lambda i: (0, 0))
    one = pl.BlockSpec((1, 1), lambda i: (0, 0))
    return _pcall(body, name=name, grid=(S // tm,),
                  in_specs=[pl.BlockSpec((tm, F), lambda i: (i, 0)), pl.BlockSpec((F, D), lambda i: (0, 0)), row, vec, row],
                  out_specs=[one, row, vec],
                  out_shape=[jax.ShapeDtypeStruct((1, 1), F32), jax.ShapeDtypeStruct((S, D), F32),
                             jax.ShapeDtypeStruct((1, D), F32)],
                  compiler_params=_params("arbitrary"))(act, w_down, x1, g, tgt)


def _sigmoid(z):
    return 1.0 / (1.0 + jnp.exp(-z))


def _gated_mix_out(gl, bg, ya, yb, w_out, x, g, *, name, tm=512):
    S, D = ya.shape

    def body(za_ref, zb_ref, ba_ref, bb_ref, ya_ref, yb_ref, w_ref, x_ref, g_ref, m_ref, x1_ref, h_ref):
        ga = _sigmoid(za_ref[...].astype(F32) + ba_ref[...])
        gb = _sigmoid(zb_ref[...].astype(F32) + bb_ref[...])
        merged = (ga * ya_ref[...].astype(F32) + gb * yb_ref[...].astype(F32)).astype(m_ref.dtype)
        m_ref[...] = merged
        x1 = x_ref[...] + jnp.dot(merged, w_ref[...].astype(_CD), preferred_element_type=F32)
        x1_ref[...] = x1
        rs = lax.rsqrt(jnp.mean(x1 * x1, axis=-1, keepdims=True) + RMS_EPS)
        h_ref[...] = ((x1 * rs) * g_ref[...]).astype(h_ref.dtype)

    lo = pl.BlockSpec((tm, D), lambda i: (i, 0))
    hi = pl.BlockSpec((tm, D), lambda i: (i, 1))
    vlo = pl.BlockSpec((1, D), lambda i: (0, 0))
    vhi = pl.BlockSpec((1, D), lambda i: (0, 1))
    whole = pl.BlockSpec((D, D), lambda i: (0, 0))
    return _pcall(body, name=name, grid=(S // tm,), in_specs=[lo, hi, vlo, vhi, lo, lo, whole, lo, vlo],
                  out_specs=[lo, lo, lo],
                  out_shape=[jax.ShapeDtypeStruct((S, D), _CD), jax.ShapeDtypeStruct((S, D), F32),
                             jax.ShapeDtypeStruct((S, D), _CD)],
                  compiler_params=_params("parallel"))(gl, gl, bg, bg, ya, yb, w_out, x, g)


def _gate_bwd(dx1, w_out, gl, bg, ya, yb, *, name, tm=512):
    S, D = ya.shape
    nt = (((1,), (1,)), ((), ()))

    def body(dx_ref, w_ref, za_ref, zb_ref, ba_ref, bb_ref, ya_ref, yb_ref, dya_ref, dyb_ref, dgl_ref, dbg_ref):
        dmv = lax.dot_general(dx_ref[...].astype(_CD), w_ref[...].astype(_CD), nt, preferred_element_type=F32)
        ga = _sigmoid(za_ref[...].astype(F32) + ba_ref[...])
        gb = _sigmoid(zb_ref[...].astype(F32) + bb_ref[...])
        dya_ref[...] = (dmv * ga).astype(dya_ref.dtype)
        dyb_ref[...] = (dmv * gb).astype(dyb_ref.dtype)
        dza = dmv * ya_ref[...].astype(F32) * ga * (1.0 - ga)
        dzb = dmv * yb_ref[...].astype(F32) * gb * (1.0 - gb)
        dgl_ref[:, :D] = dza.astype(dgl_ref.dtype)
        dgl_ref[:, D:] = dzb.astype(dgl_ref.dtype)
        pa = jnp.sum(dza, axis=0, keepdims=True)
        pb = jnp.sum(dzb, axis=0, keepdims=True)

        @pl.when(pl.program_id(0) == 0)
        def _():
            dbg_ref[:, :D] = pa
            dbg_ref[:, D:] = pb

        @pl.when(pl.program_id(0) > 0)
        def _():
            dbg_ref[:, :D] += pa
            dbg_ref[:, D:] += pb

    lo = pl.BlockSpec((tm, D), lambda i: (i, 0))
    hi = pl.BlockSpec((tm, D), lambda i: (i, 1))
    vlo = pl.BlockSpec((1, D), lambda i: (0, 0))
    vhi = pl.BlockSpec((1, D), lambda i: (0, 1))
    wide = pl.BlockSpec((tm, 2 * D), lambda i: (i, 0))
    vwide = pl.BlockSpec((1, 2 * D), lambda i: (0, 0))
    whole = pl.BlockSpec((D, D), lambda i: (0, 0))
    return _pcall(body, name=name, grid=(S // tm,), in_specs=[lo, whole, lo, hi, vlo, vhi, lo, lo],
                  out_specs=[lo, lo, wide, vwide],
                  out_shape=[jax.ShapeDtypeStruct((S, D), _CD), jax.ShapeDtypeStruct((S, D), _CD),
                             jax.ShapeDtypeStruct((S, 2 * D), _CD), jax.ShapeDtypeStruct((1, 2 * D), F32)],
                  compiler_params=_params("arbitrary"))(dx1, w_out, gl, gl, bg, bg, ya, yb)


def _ffn_in_act(h2, w_blocks, *, name, tm=512):
    S, D = h2.shape
    _, _, C = w_blocks.shape

    def body(a_ref, bg_ref, bu_ref, g_ref, u_ref, o_ref):
        av = a_ref[...].astype(_CD)
        gv = jnp.dot(av, bg_ref[0].astype(_CD), preferred_element_type=F32)
        uv = jnp.dot(av, bu_ref[0].astype(_CD), preferred_element_type=F32)
        g_ref[...] = gv.astype(g_ref.dtype)
        u_ref[...] = uv.astype(u_ref.dtype)
        o_ref[...] = (gv * _sigmoid(gv) * uv).astype(o_ref.dtype)

    out = pl.BlockSpec((tm, C), lambda i, j: (i, j))
    shp = jax.ShapeDtypeStruct((S, 2 * C), _CD)
    return _pcall(body, name=name, grid=(S // tm, 2),
                  in_specs=[pl.BlockSpec((tm, D), lambda i, j: (i, 0)), pl.BlockSpec((1, D, C), lambda i, j: (j, 0, 0)),
                            pl.BlockSpec((1, D, C), lambda i, j: (2 + j, 0, 0))],
                  out_specs=[out, out, out], out_shape=[shp, shp, shp],
                  compiler_params=_params("parallel", "arbitrary"))(h2, w_blocks, w_blocks)


def _d_swiglu(dx, w_down, gate, up, *, name, tm=512, tn=1408):
    S, D = dx.shape
    F = w_down.shape[0]
    nt = (((1,), (1,)), ((), ()))

    def body(a_ref, b_ref, g_ref, u_ref, o_ref):
        dv = lax.dot_general(a_ref[...].astype(_CD), b_ref[...].astype(_CD), nt, preferred_element_type=F32)
        gv = g_ref[...].astype(F32)
        sg = _sigmoid(gv)
        o_ref[0] = (dv * u_ref[...].astype(F32) * (sg * (1.0 + gv * (1.0 - sg)))).astype(o_ref.dtype)
        o_ref[1] = (dv * (gv * sg)).astype(o_ref.dtype)

    tile = pl.BlockSpec((tm, tn), lambda i, j: (i, j))
    return _pcall(body, name=name, grid=(S // tm, F // tn),
                  in_specs=[pl.BlockSpec((tm, D), lambda i, j: (i, 0)), pl.BlockSpec((tn, D), lambda i, j: (j, 0)),
                            tile, tile],
                  out_specs=pl.BlockSpec((2, tm, tn), lambda i, j: (0, i, j)),
                  out_shape=jax.ShapeDtypeStruct((2, S, F), _CD),
                  compiler_params=_params("parallel", "arbitrary"))(dx, w_down, gate, up)


def _split3(x):
    hi = x.astype(jnp.bfloat16)
    r1 = x - hi.astype(F32)
    mid = r1.astype(jnp.bfloat16)
    lo = (r1 - mid.astype(F32)).astype(jnp.bfloat16)
    return hi, mid, lo


def _ones_dot_left(ones, x):
    return sum(jnp.dot(ones, p, preferred_element_type=F32) for p in _split3(x))


def _ones_dot_right(x, ones):
    return sum(jnp.dot(p, ones, preferred_element_type=F32) for p in _split3(x))


def _head_sum(x):
    n = x.shape[1]
    r = lax.broadcasted_iota(jnp.int32, (n, n), 0) // HEAD_DIM
    c = lax.broadcasted_iota(jnp.int32, (n, n), 1) // HEAD_DIM
    return _ones_dot_right(x, (r == c).astype(jnp.bfloat16))


def _log_sigmoid(z):
    e = jnp.exp(-jnp.abs(z))
    t = 1.0 + e
    log1p_e = jnp.where(t == 1.0, e, jnp.log(t) * (e / jnp.where(t == 1.0, 1.0, t - 1.0)))
    return jnp.minimum(z, 0.0) - log1p_e


def _fox_cumsum(zf, bf, *, name):
    S, W = zf.shape
    nb = S // 128

    def body(z_ref, b_ref, c_ref):
        tri = (lax.broadcasted_iota(jnp.int32, (128, 128), 0) >= lax.broadcasted_iota(jnp.int32, (128, 128), 1))
        tri = tri.astype(jnp.bfloat16)

        def step(i, carry):
            rows = pl.ds(pl.multiple_of(i * 128, 128), 128)
            lf = _log_sigmoid(z_ref[rows, :] + b_ref[...])
            cb = _ones_dot_left(tri, lf) + carry
            c_ref[rows, :] = cb
            return cb[127:128, :]

        lax.fori_loop(0, nb, step, jnp.zeros((1, W), F32))

    return _pcall(body, name=name, out_shape=jax.ShapeDtypeStruct((S, W), F32),
                  compiler_params=pltpu.CompilerParams(vmem_limit_bytes=VMEM_LIMIT))(zf, bf)


def _fox_cumsum_bwd(dc, zf, bf, *, name):
    S, W = zf.shape
    nb = S // 128

    def body(dc_ref, z_ref, b_ref, dz_ref, db_ref):
        tri = (lax.broadcasted_iota(jnp.int32, (128, 128), 0) <= lax.broadcasted_iota(jnp.int32, (128, 128), 1))
        tri = tri.astype(jnp.bfloat16)

        def step(k, carry):
            tail, acc = carry
            i = nb - 1 - k
            rows = pl.ds(pl.multiple_of(i * 128, 128), 128)
            dlf = _ones_dot_left(tri, dc_ref[rows, :]) + tail
            dz = dlf * _sigmoid(-(z_ref[rows, :] + b_ref[...]))
            dz_ref[rows, :] = dz
            return dlf[0:1, :], acc + jnp.sum(dz, axis=0, keepdims=True)

        _, acc = lax.fori_loop(0, nb, step, (jnp.zeros((1, W), F32), jnp.zeros((1, W), F32)))
        db_ref[...] = acc

    return _pcall(body, name=name,
                  out_shape=[jax.ShapeDtypeStruct((S, W), F32), jax.ShapeDtypeStruct((1, W), F32)],
                  compiler_params=pltpu.CompilerParams(vmem_limit_bytes=VMEM_LIMIT))(dc, zf, bf)


def _proj_dil(h, w_qkv, *, name, tm=1024):
    S, D = h.shape
    tn = DIL_WIDTH

    def body(a_ref, b_ref, *rest):
        outs, acc = rest[:N_DIL_GROUPS], rest[N_DIL_GROUPS]
        prod = jnp.dot(a_ref[...].astype(_CD), b_ref[...].astype(_CD), preferred_element_type=F32)
        for k in range(tn // LANES):
            acc[k] = prod[:, k * LANES:(k + 1) * LANES]
        for g, (_, d) in enumerate(DIL_PAIRS):
            for half in range(DIL_OUT // LANES):
                k = g * (DIL_OUT // LANES) + half
                cols = slice(half * LANES, (half + 1) * LANES)
                for r in range(d):
                    rows = pl.ds(r, tm // d, stride=d) if d > 1 else slice(None)
                    outs[g][0, r, :, cols] = acc[k, rows, :].astype(outs[g].dtype)

    out_specs = [pl.BlockSpec((1, d, tm // d, DIL_OUT), lambda i, j: (j, 0, i, 0)) for _, d in DIL_PAIRS]
    out_shape = [jax.ShapeDtypeStruct((3, d, S // d, DIL_OUT), _CD) for _, d in DIL_PAIRS]
    outs = _pcall(body, name=name, grid=(S // tm, 3),
                  in_specs=[pl.BlockSpec((tm, D), lambda i, j: (i, 0)), pl.BlockSpec((D, tn), lambda i, j: (0, j))],
                  out_specs=out_specs, out_shape=out_shape, scratch_shapes=[pltpu.VMEM((tn // LANES, tm, LANES), F32)],
                  compiler_params=_params("parallel", "arbitrary"))(h, w_qkv)
    return [o.reshape(3, S, DIL_OUT) for o in outs]


def _dil_start(block, S, dilation):
    sub = S // dilation
    u0 = block * DIL_W
    return (u0 % sub) * dilation + u0 // sub


def _dil_slopes(group):
    h = np.arange(1, N_DIL_GROUPS * DIL_HEADS + 1, dtype=np.float32)
    s = (np.float32(2.0) ** (np.float32(-8.0) * h / np.float32(N_DIL_GROUPS * DIL_HEADS))).astype(np.float32)
    return [float(v) for v in s.reshape(N_DIL_GROUPS, DIL_HEADS)[group]]


def _dil_tiles(i, n, blocks_per_seq):
    qi = lax.broadcasted_iota(jnp.int32, (DIL_W, 2 * DIL_W), 0)
    kj = lax.broadcasted_iota(jnp.int32, (DIL_W, 2 * DIL_W), 1)
    rel = qi + DIL_W - kj
    first = ((4 * n + i) % blocks_per_seq) == 0
    valid = jnp.logical_and(jnp.logical_and(rel >= 0, rel <= DIL_W), jnp.logical_or(kj >= DIL_W, jnp.logical_not(first)))
    return valid, rel.astype(F32)


def _dil_window(cur_ref, prev_ref, i, cols):
    if i > 0:
        return cur_ref[(i - 1) * DIL_W:(i + 1) * DIL_W, cols]
    return jnp.concatenate([prev_ref[:, cols], cur_ref[:DIL_W, cols]], axis=0)


CHUNK = 4 * DIL_W


def _dil_rows(block, S, dilation):
    start = _dil_start(block, S, dilation)
    return pl.ds(start, DIL_W, stride=dilation) if dilation > 1 else pl.ds(start, DIL_W)


def SPLIT(S):
    return (DIL_OUT // LANES, S, LANES)


def _dil_fwd(qkv, group, *, name):
    S = qkv.shape[1]
    dilation = DIL_PAIRS[group][1]
    bps = (S // dilation) // DIL_W
    slopes = _dil_slopes(group)
    nt = (((1,), (1,)), ((), ()))

    def body(q_ref, k_ref, v_ref, kp_ref, vp_ref, on_ref, ln_ref, o_ref, l_ref):
        n = pl.program_id(0)
        for i in range(4):
            valid, rel = _dil_tiles(i, n, bps)
            rows = slice(i * DIL_W, (i + 1) * DIL_W)
            for h in range(DIL_HEADS):
                cols = slice(h * HEAD_DIM, (h + 1) * HEAD_DIM)
                qh = q_ref[rows, cols]
                k2, v2 = _dil_window(k_ref, kp_ref, i, cols), _dil_window(v_ref, vp_ref, i, cols)
                s = lax.dot_general(qh, k2, nt, preferred_element_type=F32) * ATTN_SCALE - (slopes[h] * dilation) * rel
                s = jnp.where(valid, s, NEG_INF)
                m = jnp.max(s, axis=-1, keepdims=True)
                p = jnp.exp(s - m)
                den = jnp.sum(p, axis=-1, keepdims=True)
                acc = jnp.dot(p.astype(_CD), v2, preferred_element_type=F32)
                o_ref[rows, cols] = acc / den
                l_ref[rows, cols] = jnp.broadcast_to(m + jnp.log(den), (DIL_W, HEAD_DIM))
        for i in range(4):
            rows = slice(i * DIL_W, (i + 1) * DIL_W)
            nat = _dil_rows(4 * n + i, S, dilation)
            for half in range(DIL_OUT // LANES):
                cols = slice(half * LANES, (half + 1) * LANES)
                on_ref[half, nat, :] = o_ref[rows, cols]
                ln_ref[half, nat, :] = l_ref[rows, cols]

    def cur(which):
        return pl.BlockSpec((None, CHUNK, DIL_OUT), lambda n: (which, n, 0))

    def prev(which):
        return pl.BlockSpec((None, DIL_W, DIL_OUT), lambda n: (which, jnp.maximum(4 * n - 1, 0), 0))

    whole = pl.BlockSpec(SPLIT(S), lambda n: (0, 0, 0))
    return _pcall(body, name=name, grid=(S // CHUNK,), in_specs=[cur(0), cur(1), cur(2), prev(1), prev(2)],
                  out_specs=[whole, whole],
                  out_shape=[jax.ShapeDtypeStruct(SPLIT(S), F32), jax.ShapeDtypeStruct(SPLIT(S), F32)],
                  scratch_shapes=[pltpu.VMEM((CHUNK, DIL_OUT), F32), pltpu.VMEM((CHUNK, DIL_OUT), F32)],
                  compiler_params=_params("arbitrary"))(qkv, qkv, qkv, qkv, qkv)


STAT_OFFSET = HEAD_DIM // 2


def _dil_bwd(qkv, stats, do, group, *, name):
    S = qkv.shape[1]
    dilation = DIL_PAIRS[group][1]
    bps = (S // dilation) // DIL_W
    slopes = _dil_slopes(group)
    nchunk = S // CHUNK
    nt = (((1,), (1,)), ((), ()))
    tn = (((0,), (0,)), ((), ()))

    def body(q_ref, k_ref, v_ref, kp_ref, vp_ref, ln_ref, don_ref, dqn_ref, dkn_ref, dvn_ref,
             dk_s, dv_s, l_ref, do_ref, dq_ref):
        step = pl.program_id(0)
        n = nchunk - 1 - step
        for i in range(4):
            rows = slice(i * DIL_W, (i + 1) * DIL_W)
            nat = _dil_rows(4 * n + i, S, dilation)
            for half in range(DIL_OUT // LANES):
                cols = slice(half * LANES, (half + 1) * LANES)
                l_ref[rows, cols] = ln_ref[half, nat, :]
                do_ref[rows, cols] = don_ref[half, nat, :]

        @pl.when(step == 0)
        def _():
            dk_s[:, CHUNK:] = jnp.zeros((DIL_OUT, DIL_W), F32)
            dv_s[:, CHUNK:] = jnp.zeros((DIL_OUT, DIL_W), F32)

        dk_s[:, :CHUNK] = jnp.zeros((DIL_OUT, CHUNK), F32)
        dv_s[:, :CHUNK] = jnp.zeros((DIL_OUT, CHUNK), F32)
        for i in range(4):
            valid, rel = _dil_tiles(i, n, bps)
            rows = slice(i * DIL_W, (i + 1) * DIL_W)
            window = slice(i * DIL_W, (i + 2) * DIL_W)
            for h in range(DIL_HEADS):
                cols = slice(h * HEAD_DIM, (h + 1) * HEAD_DIM)
                qh = q_ref[rows, cols]
                k2, v2 = _dil_window(k_ref, kp_ref, i, cols), _dil_window(v_ref, vp_ref, i, cols)
                lh = l_ref[rows, h * HEAD_DIM:h * HEAD_DIM + 1]
                shift = l_ref[rows, h * HEAD_DIM + STAT_OFFSET:h * HEAD_DIM + STAT_OFFSET + 1]
                s = lax.dot_general(qh, k2, nt, preferred_element_type=F32) * ATTN_SCALE - (slopes[h] * dilation) * rel
                p = jnp.exp(jnp.where(valid, s, NEG_INF) - lh)
                dob = do_ref[rows, cols].astype(_CD)
                ds = p * (lax.dot_general(dob, v2, nt, preferred_element_type=F32) + shift)
                dsb = (ds * ATTN_SCALE).astype(_CD)
                dq_ref[rows, cols] = jnp.dot(dsb, k2, preferred_element_type=F32)
                dk_s[cols, window] += lax.dot_general(qh, dsb, tn, preferred_element_type=F32)
                dv_s[cols, window] += lax.dot_general(dob, p.astype(_CD), tn, preferred_element_type=F32)
        for i in range(4):
            rows = slice(i * DIL_W, (i + 1) * DIL_W)
            done = slice((i + 1) * DIL_W, (i + 2) * DIL_W)
            nat = _dil_rows(4 * n + i, S, dilation)
            dkb, dvb = dk_s[:, done].T, dv_s[:, done].T
            for half in range(DIL_OUT // LANES):
                cols = slice(half * LANES, (half + 1) * LANES)
                dqn_ref[half, nat, :] = dq_ref[rows, cols]
                dkn_ref[half, nat, :] = dkb[:, cols]
                dvn_ref[half, nat, :] = dvb[:, cols]
        dk_s[:, CHUNK:] = dk_s[:, :DIL_W]
        dv_s[:, CHUNK:] = dv_s[:, :DIL_W]

    def cur(which):
        return pl.BlockSpec((None, CHUNK, DIL_OUT), lambda s: (which, nchunk - 1 - s, 0))

    def prev(which):
        return pl.BlockSpec((None, DIL_W, DIL_OUT), lambda s: (which, jnp.maximum(4 * (nchunk - 1 - s) - 1, 0), 0))

    whole = pl.BlockSpec(SPLIT(S), lambda s: (0, 0, 0))
    shp = jax.ShapeDtypeStruct(SPLIT(S), F32)
    tile = pltpu.VMEM((CHUNK, DIL_OUT), F32)
    return _pcall(body, name=name, grid=(nchunk,),
                  in_specs=[cur(0), cur(1), cur(2), prev(1), prev(2), whole, whole],
                  out_specs=[whole, whole, whole], out_shape=[shp, shp, shp],
                  scratch_shapes=[pltpu.VMEM((DIL_OUT, CHUNK + DIL_W), F32), pltpu.VMEM((DIL_OUT, CHUNK + DIL_W), F32),
                                  tile, tile, tile],
                  compiler_params=pltpu.CompilerParams(dimension_semantics=("arbitrary",),
                                                       vmem_limit_bytes=VMEM_LIMIT_RESIDENT))(
        qkv, qkv, qkv, qkv, qkv, stats, do)


def _dil_mix_fwd(os_, ls_, *, name, tm=512):
    nh, S, _ = os_[0].shape

    def body(o0, o1, o2, l0, l1, l2, out_ref):
        for half in range(nh):
            ls = [l0[half], l1[half], l2[half]]
            m = jnp.maximum(jnp.maximum(ls[0], ls[1]), ls[2])
            es = [jnp.exp(l - m) for l in ls]
            den = es[0] + es[1] + es[2]
            mixed = (es[0] * o0[half] + es[1] * o1[half] + es[2] * o2[half]) / den
            out_ref[:, half * LANES:(half + 1) * LANES] = mixed.astype(out_ref.dtype)

    halves = pl.BlockSpec((nh, tm, LANES), lambda i: (0, i, 0))
    row = pl.BlockSpec((tm, nh * LANES), lambda i: (i, 0))
    return _pcall(body, name=name, grid=(S // tm,), in_specs=[halves] * 6, out_specs=row,
                  out_shape=jax.ShapeDtypeStruct((S, nh * LANES), _CD), compiler_params=_params("parallel"))(*os_, *ls_)


def _dil_mix_bwd(doa, os_, ls_, *, name, tm=512, after=None):
    nh, S, _ = os_[0].shape

    def body(d_ref, o0, o1, o2, l0, l1, l2, do0, do1, do2, st0, st1, st2):
        first = lax.broadcasted_iota(jnp.int32, (tm, LANES), 1) % HEAD_DIM < STAT_OFFSET
        for half in range(nh):
            dv = d_ref[:, half * LANES:(half + 1) * LANES]
            ls = [l0[half], l1[half], l2[half]]
            m = jnp.maximum(jnp.maximum(ls[0], ls[1]), ls[2])
            es = [jnp.exp(l - m) for l in ls]
            den = es[0] + es[1] + es[2]
            al = [e / den for e in es]
            da = [_head_sum(dv * o[half]) for o in (o0, o1, o2)]
            mean = al[0] * da[0] + al[1] * da[1] + al[2] * da[2]
            for a, l, do_ref, st_ref in zip(al, ls, (do0, do1, do2), (st0, st1, st2)):
                do_ref[half] = a * dv
                st_ref[half] = jnp.where(first, l, -a * mean)

    halves = pl.BlockSpec((nh, tm, LANES), lambda i: (0, i, 0))
    row = pl.BlockSpec((tm, nh * LANES), lambda i: (i, 0))
    shp = jax.ShapeDtypeStruct((nh, S, LANES), F32)
    return _pcall(body, after, name=name, grid=(S // tm,), in_specs=[row] + [halves] * 6, out_specs=[halves] * 6,
                  out_shape=[shp] * 6, compiler_params=_params("parallel"))(doa, *os_, *ls_)


FOX_T = 512


PACK = 2 * HEAD_DIM
HEAD_PAIRS = N_FOX_HEADS // 2
FOX_HPS = 8
Q_BLOCK0 = 0
K_BLOCK0 = FOX_WIDTH // PACK
V_BLOCK0 = 2 * FOX_WIDTH // PACK


def _pieces(x):
    hi = x.astype(jnp.bfloat16).astype(F32)
    r = x - hi
    mid = r.astype(jnp.bfloat16).astype(F32)
    lo = (r - mid).astype(jnp.bfloat16).astype(F32)
    return [hi, mid, lo]


def _extras(first, second, rows):
    lane = lax.broadcasted_iota(jnp.int32, (rows, HEAD_DIM), 1)
    out = jnp.zeros((rows, HEAD_DIM), F32)
    for base, triple in ((0, first), (3, second)):
        if all(isinstance(v, float) for v in triple) and len(set(triple)) == 1:
            if triple[0] != 0.0:
                out = jnp.where(jnp.logical_and(lane >= base, lane < base + 3), triple[0], out)
        else:
            for idx, val in enumerate(triple):
                out = jnp.where(lane == base + idx, val, out)
    return out


def _head_column(c, h):
    lane = lax.broadcasted_iota(jnp.int32, c.shape, 1)
    return jnp.sum(jnp.where(lane == h, c, 0.0), axis=1, keepdims=True)


ONES3 = [1.0, 1.0, 1.0]
ZEROS3 = [0.0, 0.0, 0.0]


def _fox_pack_fwd(qkv, c, *, name, tm=1024):
    S = qkv.shape[0]

    def body(q_ref, k_ref, v_ref, c_ref, qo_ref, ko_ref, vo_ref):
        hp = pl.program_id(1)
        cv = c_ref[...]
        v_extras = jnp.where(lax.broadcasted_iota(jnp.int32, (tm, HEAD_DIM), 1) < 3, 1.0, 0.0).astype(vo_ref.dtype)
        for hh in range(2):
            ch = _pieces(_head_column(cv, 2 * hp + hh))
            src = slice(hh * HEAD_DIM, (hh + 1) * HEAD_DIM)
            lo = slice(hh * PACK, hh * PACK + HEAD_DIM)
            hi = slice(hh * PACK + HEAD_DIM, (hh + 1) * PACK)
            qo_ref[:, lo] = (q_ref[:, src].astype(F32) * ATTN_SCALE).astype(qo_ref.dtype)
            qo_ref[:, hi] = _extras(ch, ONES3, tm).astype(qo_ref.dtype)
            ko_ref[:, lo] = k_ref[:, src]
            ko_ref[:, hi] = _extras(ONES3, [-p for p in ch], tm).astype(ko_ref.dtype)
            vo_ref[:, lo] = v_ref[:, src]
            vo_ref[:, hi] = v_extras

    def src(block0):
        return pl.BlockSpec((tm, PACK), lambda i, hp: (i, block0 + hp))

    out = pl.BlockSpec((tm, 2 * PACK), lambda i, hp: (i, hp))
    shp = jax.ShapeDtypeStruct((S, N_FOX_HEADS * PACK), _CD)
    return _pcall(body, name=name, grid=(S // tm, HEAD_PAIRS),
                  in_specs=[src(Q_BLOCK0), src(K_BLOCK0), src(V_BLOCK0), pl.BlockSpec((tm, PACK), lambda i, hp: (i, 0))],
                  out_specs=[out, out, out], out_shape=[shp, shp, shp],
                  compiler_params=_params("parallel", "parallel"))(qkv, qkv, qkv, c)


def _fox_fwd(qp, kp, vp, *, name):
    S = qp.shape[0]
    nt = S // FOX_T
    nt_dims = (((1,), (1,)), ((), ()))
    tn_dims = (((0,), (0,)), ((), ()))

    def body(i_tab, j_tab, q_ref, k_ref, v_ref, o_ref, l_ref, m_s, acc_s):
        t = pl.program_id(1)
        i, j = i_tab[t], j_tab[t]

        @pl.when(j == 0)
        def _():
            m_s[...] = jnp.full((FOX_HPS, 1, FOX_T), NEG_INF, F32)
            acc_s[...] = jnp.zeros((FOX_HPS, PACK, FOX_T), F32)

        def tile(diagonal):
            for hh in range(FOX_HPS):
                cols = slice(hh * PACK, (hh + 1) * PACK)
                st = lax.dot_general(k_ref[:, cols], q_ref[:, cols], nt_dims, preferred_element_type=F32)
                if diagonal:
                    key = lax.broadcasted_iota(jnp.int32, (FOX_T, FOX_T), 0)
                    qry = lax.broadcasted_iota(jnp.int32, (FOX_T, FOX_T), 1)
                    st = jnp.where(key <= qry, st, NEG_INF)
                m_old = m_s[hh]
                m_new = jnp.maximum(m_old, jnp.max(st, axis=0, keepdims=True))
                pt = jnp.exp(st - m_new)
                acc_s[hh] = jnp.exp(m_old - m_new) * acc_s[hh] + lax.dot_general(
                    v_ref[:, cols], pt.astype(_CD), tn_dims, preferred_element_type=F32)
                m_s[hh] = m_new

        @pl.when(j < i)
        def _():
            tile(False)

        @pl.when(j == i)
        def _():
            tile(True)
            for hh in range(FOX_HPS):
                acc = acc_s[hh]
                den = acc[HEAD_DIM:HEAD_DIM + 1, :]
                cols = slice(hh * HEAD_DIM, (hh + 1) * HEAD_DIM)
                o_ref[:, cols] = (acc[:HEAD_DIM, :] / den).T
                l_ref[:, cols] = jnp.broadcast_to(m_s[hh] + jnp.log(den), (HEAD_DIM, FOX_T)).T

    pairs = [(i, j) for i in range(nt) for j in range(i + 1)]
    i_tab = jnp.asarray([p[0] for p in pairs], jnp.int32)
    j_tab = jnp.asarray([p[1] for p in pairs], jnp.int32)
    qs = pl.BlockSpec((FOX_T, FOX_HPS * PACK), lambda hp, t, it, jt: (it[t], hp))
    ks = pl.BlockSpec((FOX_T, FOX_HPS * PACK), lambda hp, t, it, jt: (jt[t], hp))
    os_ = pl.BlockSpec((FOX_T, FOX_HPS * HEAD_DIM), lambda hp, t, it, jt: (it[t], hp))
    shp = jax.ShapeDtypeStruct((S, FOX_WIDTH), F32)
    grid_spec = pltpu.PrefetchScalarGridSpec(
        num_scalar_prefetch=2, grid=(N_FOX_HEADS // FOX_HPS, len(pairs)), in_specs=[qs, ks, ks], out_specs=[os_, os_],
        scratch_shapes=[pltpu.VMEM((FOX_HPS, 1, FOX_T), F32), pltpu.VMEM((FOX_HPS, PACK, FOX_T), F32)])
    return _pcall(body, name=name, grid_spec=grid_spec, out_shape=[shp, shp],
                  compiler_params=_params("parallel", "arbitrary"))(i_tab, j_tab, qp, kp, vp)


def _fox_pack_bwd(qkv, c, o, lse, do, *, name, tm=1024, after=None):
    S = qkv.shape[0]

    def body(q_ref, c_ref, o_ref, l_ref, do_ref, qo_ref, do_out_ref):
        hp = pl.program_id(1)
        cv = c_ref[...]
        for hh in range(2):
            src = slice(hh * HEAD_DIM, (hh + 1) * HEAD_DIM)
            lo = slice(hh * PACK, hh * PACK + HEAD_DIM)
            hi = slice(hh * PACK + HEAD_DIM, (hh + 1) * PACK)
            shift = _head_column(cv, 2 * hp + hh) - l_ref[:, hh * HEAD_DIM:hh * HEAD_DIM + 1]
            dov = do_ref[:, src]
            dsum = jnp.sum(dov * o_ref[:, src], axis=-1, keepdims=True)
            qo_ref[:, lo] = (q_ref[:, src].astype(F32) * ATTN_SCALE).astype(qo_ref.dtype)
            qo_ref[:, hi] = _extras(_pieces(shift), ONES3, tm).astype(qo_ref.dtype)
            do_out_ref[:, lo] = dov.astype(do_out_ref.dtype)
            do_out_ref[:, hi] = _extras(_pieces(-dsum), ZEROS3, tm).astype(do_out_ref.dtype)

    pair = pl.BlockSpec((tm, PACK), lambda i, hp: (i, hp))
    out = pl.BlockSpec((tm, 2 * PACK), lambda i, hp: (i, hp))
    shp = jax.ShapeDtypeStruct((S, N_FOX_HEADS * PACK), _CD)
    return _pcall(body, after, name=name, grid=(S // tm, HEAD_PAIRS),
                  in_specs=[pl.BlockSpec((tm, PACK), lambda i, hp: (i, Q_BLOCK0 + hp)),
                            pl.BlockSpec((tm, PACK), lambda i, hp: (i, 0)), pair, pair, pair],
                  out_specs=[out, out], out_shape=[shp, shp],
                  compiler_params=_params("parallel", "parallel"))(qkv, c, o, lse, do)


def _fox_bwd(qp, kp, vp, dop, *, name):
    S = qp.shape[0]
    nt = S // FOX_T
    nt_dims = (((1,), (1,)), ((), ()))
    tn_dims = (((0,), (0,)), ((), ()))

    def body(i_tab, j_tab, q_ref, k_ref, v_ref, do_ref, dq_ref, dk_ref, dv_ref, dc_ref, dr_ref,
             dq_s, dk_s, dv_s, dc_s, dr_s):
        t = pl.program_id(1)
        i, j = i_tab[t], j_tab[t]

        @pl.when(t == 0)
        def _():
            dq_s[...] = jnp.zeros((S, FOX_HPS * PACK), F32)
            dr_s[...] = jnp.zeros((FOX_HPS, 1, S), F32)

        @pl.when(i == j)
        def _():
            dk_s[...] = jnp.zeros((FOX_T, FOX_HPS * PACK), F32)
            dv_s[...] = jnp.zeros((FOX_T, FOX_HPS * PACK), F32)
            dc_s[...] = jnp.zeros((FOX_HPS, FOX_T, 1), F32)

        def tile(diagonal):
            rows = pl.ds(pl.multiple_of(i * FOX_T, FOX_T), FOX_T)
            for hh in range(FOX_HPS):
                cols = slice(hh * PACK, (hh + 1) * PACK)
                qv, kv, vv, dov = q_ref[:, cols], k_ref[:, cols], v_ref[:, cols], do_ref[:, cols]
                pt = jnp.exp(lax.dot_general(kv, qv, nt_dims, preferred_element_type=F32))
                if diagonal:
                    key = lax.broadcasted_iota(jnp.int32, (FOX_T, FOX_T), 0)
                    qry = lax.broadcasted_iota(jnp.int32, (FOX_T, FOX_T), 1)
                    pt = jnp.where(key <= qry, pt, 0.0)
                dst = pt * lax.dot_general(vv, dov, nt_dims, preferred_element_type=F32)
                dsb = dst.astype(_CD)
                dc_s[hh] += jnp.sum(dst, axis=1, keepdims=True)
                dr_s[hh, :, rows] += jnp.sum(dst, axis=0, keepdims=True)
                dv_s[:, cols] += jnp.dot(pt.astype(_CD), dov, preferred_element_type=F32)
                dk_s[:, cols] += jnp.dot(dsb, qv, preferred_element_type=F32)
                dq_s[rows, cols] += lax.dot_general(dsb, kv, tn_dims, preferred_element_type=F32)

        @pl.when(i > j)
        def _():
            tile(False)

        @pl.when(i == j)
        def _():
            tile(True)

        @pl.when(i == nt - 1)
        def _():
            for hh in range(FOX_HPS):
                src = slice(hh * PACK, hh * PACK + HEAD_DIM)
                dst_cols = slice(hh * HEAD_DIM, (hh + 1) * HEAD_DIM)
                dk_ref[:, dst_cols] = dk_s[:, src].astype(dk_ref.dtype)
                dv_ref[:, dst_cols] = dv_s[:, src].astype(dv_ref.dtype)
                dc_ref[:, dst_cols] = jnp.broadcast_to(dc_s[hh], (FOX_T, HEAD_DIM))

        @pl.when(t == len(pairs) - 1)
        def _():
            for hh in range(FOX_HPS):
                dq_ref[:, hh * HEAD_DIM:(hh + 1) * HEAD_DIM] = (
                    dq_s[:, hh * PACK:hh * PACK + HEAD_DIM] * ATTN_SCALE).astype(dq_ref.dtype)
            dr_ref[...] = dr_s[...]

    pairs = [(i, j) for j in range(nt) for i in range(j, nt)]
    i_tab = jnp.asarray([p[0] for p in pairs], jnp.int32)
    j_tab = jnp.asarray([p[1] for p in pairs], jnp.int32)
    wide, narrow = FOX_HPS * PACK, FOX_HPS * HEAD_DIM
    qs = pl.BlockSpec((FOX_T, wide), lambda hp, t, it, jt: (it[t], hp))
    ks = pl.BlockSpec((FOX_T, wide), lambda hp, t, it, jt: (jt[t], hp))
    whole = pl.BlockSpec((S, narrow), lambda hp, t, it, jt: (0, hp))
    cs = pl.BlockSpec((FOX_T, narrow), lambda hp, t, it, jt: (jt[t], hp))
    rs = pl.BlockSpec((FOX_HPS, 1, S), lambda hp, t, it, jt: (hp, 0, 0))
    shp = jax.ShapeDtypeStruct((S, FOX_WIDTH), _CD)
    grid_spec = pltpu.PrefetchScalarGridSpec(
        num_scalar_prefetch=2, grid=(N_FOX_HEADS // FOX_HPS, len(pairs)), in_specs=[qs, ks, ks, qs],
        out_specs=[whole, cs, cs, cs, rs],
        scratch_shapes=[pltpu.VMEM((S, wide), F32), pltpu.VMEM((FOX_T, wide), F32),
                        pltpu.VMEM((FOX_T, wide), F32), pltpu.VMEM((FOX_HPS, FOX_T, 1), F32),
                        pltpu.VMEM((FOX_HPS, 1, S), F32)])
    return _pcall(body, name=name, grid_spec=grid_spec,
                  out_shape=[shp, shp, shp, jax.ShapeDtypeStruct((S, FOX_WIDTH), F32),
                             jax.ShapeDtypeStruct((N_FOX_HEADS, 1, S), F32)],
                  compiler_params=_params("parallel", "arbitrary"))(i_tab, j_tab, qp, kp, vp, dop)


def _layer_step(x, tgt, w, p, late_weights=None, grad_sink=None, after=None, first_weights=None):
    S = x.shape[0]
    after_norm, after_proj = after if after is not None else (None, None)
    h = _rms_fwd(x, p["norm_mix_g"], name="rms_mix", after=after_norm)
    if first_weights is not None:
        w = {**w, **first_weights(h)}
    qkv = _mm(h, w["qkv"][:, 3 * DIL_WIDTH:], name="proj_fox", out_dtype=_CD, tn=768, tm=2048, after=after_proj)
    dil_qkv = _proj_dil(h, w["qkv"], name="proj_dil")
    zf = _mm(h, w["f"], name="proj_f")
    gl = _mm(h, w["g"], name="proj_gate", tn=1024, out_dtype=_CD)

    dil_o, dil_l = [], []
    for g in range(N_DIL_GROUPS):
        og, lg = _dil_fwd(dil_qkv[g], g, name=f"dil_fwd{g}")
        dil_o.append(og), dil_l.append(lg)
    o_a = _dil_mix_fwd(dil_o, dil_l, name="dil_mix")

    c = _fox_cumsum(zf, p["b_fgt"], name="fox_cumsum")
    fqp, fkp, fvp = _fox_pack_fwd(qkv, c, name="fox_pack")
    o_b, flse = _fox_fwd(fqp, fkp, fvp, name="fox_fwd")

    if late_weights is not None:
        w = {**w, **late_weights(o_b)}
    y_a = _mm(o_a, w["dil_out"], name="y_a", tn=1024, out_dtype=_CD)
    y_b = _mm(o_b, w["fox_out"], name="y_b", tn=1024, out_dtype=_CD)
    merged, x1, h2 = _gated_mix_out(gl, p["b_gate"], y_a, y_b, w["out"], x, p["norm_ffn_g"], name="mix_out")
    gate, up, act = _ffn_in_act(h2, w["ffn_in"], name="ffn_in")
    loss, dx2, dg_final = _ffn_down_loss(act, w["ffn_down"], x1, p["norm_final_g"], tgt, name="ffn_down_loss")

    gw_ffn_down = _mm(act, dx2, name="gw_ffn_down", ta=True, out_dtype=_CD, tm=1408)
    dgu = _d_swiglu(dx2, w["ffn_down"], gate, up, name="d_swiglu")
    gw_ffn_in = _mm(h2, dgu, name="gw_ffn_in", ta=True, out_dtype=_CD, tn=1408, out_blocks=1408, b_halves=True)
    sink = grad_sink if grad_sink is not None else (lambda group, grads: None)
    tok = sink("ffn", dict(ffn_in=gw_ffn_in, ffn_down=gw_ffn_down))
    dx1, dg_ffn = _mm(dgu, w["ffn_in"], name="d_h2", tb=True, tk=1408, b_blocks=True, tm=1024, a_halves=True,
                      rms_bwd=(x1, p["norm_ffn_g"], dx2), after=tok)

    gw_out = _mm(merged, dx1, name="gw_out", ta=True, out_dtype=_CD)
    dy_a, dy_b, dgl, db_gate = _gate_bwd(dx1, w["out"], gl, p["b_gate"], y_a, y_b, name="gate_bwd")
    do_a = _mm(dy_a, w["dil_out"], name="d_o_a", tb=True)
    gw_dil_out = _mm(o_a, dy_a, name="gw_dil_out", ta=True, out_dtype=_CD, tn=1024)
    do_b = _mm(dy_b, w["fox_out"], name="d_o_b", tb=True)
    gw_fox_out = _mm(o_b, dy_b, name="gw_fox_out", ta=True, out_dtype=_CD, tn=1024)
    tok = sink("mix", dict(dil_out=gw_dil_out, fox_out=gw_fox_out, out=gw_out))

    bqp, bdop = _fox_pack_bwd(qkv, c, o_b, flse, do_b, name="fox_pack_bwd", after=tok)
    dqp, dkp, dvp, dck, dcq = _fox_bwd(bqp, fkp, fvp, bdop, name="fox_bwd")
    dc = dcq[:, 0, :].T - dck.reshape(S, N_FOX_HEADS, HEAD_DIM)[:, :, 0]
    dc = jnp.pad(dc, ((0, 0), (0, F_PAD - N_FOX_HEADS)))
    dzf, db_fgt = _fox_cumsum_bwd(dc, zf, p["b_fgt"], name="fox_cumsum_bwd")

    douts = _dil_mix_bwd(do_a, dil_o, dil_l, name="dil_mix_bwd", after=tok)
    dqs, dks, dvs = [], [], []
    for g in range(N_DIL_GROUPS):
        dq, dk, dv = _dil_bwd(dil_qkv[g], douts[3 + g], douts[g], g, name=f"dil_bwd{g}")
        for parts, t in ((dqs, dq), (dks, dk), (dvs, dv)):
            parts.extend([t[0].astype(_CD), t[1].astype(_CD)])
    dqkv = jnp.concatenate(dqs + dks + dvs + [dqp, dkp, dvp], axis=1)

    gw_qkv = _mm(h, dqkv, name="gw_qkv", ta=True, out_dtype=_CD, tn=768)
    gw_g = _mm(h, dgl, name="gw_gate", ta=True, out_dtype=_CD)
    gw_f = _mm(h, dzf, name="gw_f", ta=True, out_dtype=_CD)
    tok = sink("in", dict(qkv=gw_qkv, f=gw_f, g=gw_g))
    dx, dg_mix = _mm(dqkv, w["qkv"], name="d_h", tb=True, tk=QKV_COLS, tm=256, more=((dgl, w["g"]), (dzf, w["f"])),
                     rms_bwd=(x, p["norm_mix_g"], dx1), after=tok)

    gw = dict(qkv=gw_qkv, f=gw_f, g=gw_g, dil_out=gw_dil_out, fox_out=gw_fox_out, out=gw_out, ffn_in=gw_ffn_in,
              ffn_down=gw_ffn_down)
    small = dict(norm_mix_g=dg_mix, b_fgt=db_fgt, b_gate=db_gate, norm_ffn_g=dg_ffn, norm_final_g=dg_final)
    return loss, dx, gw, small


def _position():
    return lax.axis_index("x"), lax.axis_index("y"), lax.axis_index("c")


def _other_chips(x, y):
    return [(1 - x, y), (x, 1 - y), (1 - x, 1 - y)]


ROW_TILE = 16


def _row_chunks(rows, want=4):
    n = want
    while n > 1 and rows % (n * ROW_TILE):
        n //= 2
    return n


SEM_SPEC = pl.BlockSpec(memory_space=pltpu.SEMAPHORE)
ANY_SPEC = pl.BlockSpec(memory_space=pl.ANY)
DATAFLOW = pltpu.SideEffectType.DATAFLOW_SIDE_EFFECTING


def _in_hbm(a):
    return pltpu.with_memory_space_constraint(a, pltpu.HBM)


def _split_copy_start(srcs, land_shapes, copies, after, *, name):
    n, m = len(srcs), len(land_shapes)

    def body(*refs):
        src_refs, land_refs = refs[:n], refs[n:n + m]
        send_sems, recv_sems = refs[n + m + 1], refs[n + m + 2]
        token = refs[-1]
        x, y, c = _position()
        for k, (src, dst, peer) in enumerate(copies(x, y, c, src_refs, land_refs)):
            pltpu.make_async_remote_copy(src_ref=src, dst_ref=dst, send_sem=send_sems.at[k], recv_sem=recv_sems.at[k],
                                         device_id=peer, device_id_type=MESH).start()
        token[...] = jnp.zeros_like(token)

    lands = [lax.empty(s.shape, s.dtype) for s in land_shapes]
    count = len(copies(0, 0, 0, srcs, lands))
    out = _pcall(
        body, name=name,
        out_shape=(pltpu.SemaphoreType.DMA((count,)), pltpu.SemaphoreType.DMA((count,)),
                   *[pltpu.HBM(s.shape, s.dtype) for s in srcs], *[pltpu.HBM(s.shape, s.dtype) for s in land_shapes],
                   jax.ShapeDtypeStruct((8, 128), F32)),
        in_specs=[HBM_SPEC] * (n + m) + [ANY_SPEC],
        out_specs=(SEM_SPEC, SEM_SPEC, *[HBM_SPEC] * (n + m), pl.BlockSpec(memory_space=pltpu.VMEM)),
        input_output_aliases={k: 2 + k for k in range(n + m)},
        compiler_params=pltpu.CompilerParams(has_side_effects=DATAFLOW),
    )(*[_in_hbm(s) for s in srcs], *[_in_hbm(l) for l in lands], after)
    return out[0], out[1], list(out[2:2 + n]), list(out[2 + n:2 + n + m]), out[-1]


def _split_copy_wait(send_sems, recv_sems, srcs, lands, copies, after, *, name):
    n, m = len(srcs), len(lands)

    def body(*refs):
        src_refs, land_refs = refs[:n], refs[n:n + m]
        send, recv = refs[n + m], refs[n + m + 1]
        x, y, c = _position()
        for k, (src, dst, peer) in enumerate(copies(x, y, c, src_refs, land_refs)):
            cp = pltpu.make_async_remote_copy(src_ref=src, dst_ref=dst, send_sem=send.at[k], recv_sem=recv.at[k],
                                              device_id=peer, device_id_type=MESH)
            cp.wait_send()
            cp.wait_recv()

    afters = list(after) if isinstance(after, (list, tuple)) else [after]
    out = _pcall(
        body, name=name,
        out_shape=tuple(pltpu.HBM(s.shape, s.dtype) for s in list(srcs) + list(lands)),
        in_specs=[HBM_SPEC] * (n + m) + [SEM_SPEC, SEM_SPEC] + [ANY_SPEC] * len(afters),
        out_specs=tuple([HBM_SPEC] * (n + m)),
        input_output_aliases={k: k for k in range(n + m)},
        compiler_params=pltpu.CompilerParams(has_side_effects=DATAFLOW),
    )(*srcs, *lands, send_sems, recv_sems, *afters)
    return list(out[:n]), list(out[n:])


def _gather_copies(x, y, c, shard_refs, land_refs):
    out = []
    for s, l in zip(shard_refs, land_refs):
        half = s.shape[0] // 2
        nq = _row_chunks(half)
        for cx, cy in _other_chips(x, y):
            for q in range(nq):
                rows = pl.ds(c * half + q * (half // nq), half // nq)
                out.append((s.at[rows, :], l.at[2 * x + y, rows, :], (cx, cy, c)))
    return out


def _gather_whole_copies(x, y, c, shard_refs, land_refs):
    out = []
    for s, l in zip(shard_refs, land_refs):
        nq = _row_chunks(s.shape[0])
        for cx, cy in _other_chips(x, y):
            for q in range(nq):
                rows = pl.ds(q * (s.shape[0] // nq), s.shape[0] // nq)
                out.append((s.at[rows, :], l.at[2 * x + y, rows, :], (cx, cy, c)))
    return out


def _scatter_all_copies(x, y, c, block_refs, land_refs):
    out = []
    for g, l in zip(block_refs, land_refs):
        half = g.shape[1] // 2
        nq = _row_chunks(half)
        size = half // nq
        for q in range(nq):
            rows = pl.ds((1 - c) * half + q * size, size)
            out.append((g.at[2 * x + y, rows, :], l.at[0, pl.ds(q * size, size), :], (x, y, 1 - c)))
        for r, (cx, cy) in enumerate(_other_chips(x, y)):
            for j in range(2):
                h = c if j == 0 else 1 - c
                for q in range(nq):
                    rows = pl.ds(h * half + q * size, size)
                    out.append((g.at[2 * cx + cy, rows, :], l.at[1 + 2 * r + j, pl.ds(q * size, size), :], (cx, cy, h)))
    return out


def _forward_halves(lands, *, name):
    n = len(lands)

    def body(*refs):
        ins = refs[:n]
        send_sems, recv_sems = refs[2 * n:]
        x, y, c = _position()
        copies = []
        for w in range(n):
            half = ins[w].shape[1] // 2
            for r, (cx, cy) in enumerate(_other_chips(x, y)):
                blk = ins[w].at[2 * cx + cy, pl.ds(c * half, half), :]
                cp = pltpu.make_async_remote_copy(src_ref=blk, dst_ref=blk, send_sem=send_sems.at[w, r],
                                                  recv_sem=recv_sems.at[w, r], device_id=(x, y, 1 - c),
                                                  device_id_type=MESH)
                cp.start()
                copies.append(cp)
        for w in range(n):
            half = ins[w].shape[1] // 2
            for r, (cx, cy) in enumerate(_other_chips(x, y)):
                blk = ins[w].at[2 * cx + cy, pl.ds((1 - c) * half, half), :]
                pltpu.make_async_remote_copy(src_ref=blk, dst_ref=blk, send_sem=send_sems.at[w, r],
                                             recv_sem=recv_sems.at[w, r], device_id=(x, y, 1 - c),
                                             device_id_type=MESH).wait_recv()
        for cp in copies:
            cp.wait_send()

    return _pcall(
        body, name=name, in_specs=[HBM_SPEC] * n, out_specs=[HBM_SPEC] * n,
        out_shape=[jax.ShapeDtypeStruct(l.shape, l.dtype) for l in lands],
        input_output_aliases={k: k for k in range(n)},
        scratch_shapes=[pltpu.SemaphoreType.DMA((n, 3)), pltpu.SemaphoreType.DMA((n, 3))],
    )(*lands)


def _share_halves(halves, *, name):
    n = len(halves)

    def body(*refs):
        ins, outs = refs[:n], refs[n:2 * n]
        send_sems, recv_sems = refs[2 * n:]
        x, y, c = _position()
        copies = []
        for w in range(n):
            cp = pltpu.make_async_remote_copy(src_ref=ins[w], dst_ref=outs[w], send_sem=send_sems.at[w],
                                              recv_sem=recv_sems.at[w], device_id=(x, y, 1 - c), device_id_type=MESH)
            cp.start()
            copies.append(cp)
        for cp in copies:
            cp.wait()

    return _pcall(
        body, name=name, in_specs=[HBM_SPEC] * n, out_specs=[HBM_SPEC] * n,
        out_shape=[jax.ShapeDtypeStruct(h.shape, h.dtype) for h in halves],
        scratch_shapes=[pltpu.SemaphoreType.DMA((n,)), pltpu.SemaphoreType.DMA((n,))],
    )(*halves)


def _sum_small(part, after=None):
    rows, width = part.shape

    def body(x_ref, out_ref, all_ref, send_sems, recv_sems):
        x, y, c = _position()
        me, sibling = (x, y, c), (x, y, 1 - c)
        chips = _other_chips(x, y)

        def block(px, py, pc):
            return all_ref.at[pl.ds((4 * px + 2 * py + pc) * rows, rows), :]

        def copy(k, blk, to, src=None):
            return pltpu.make_async_remote_copy(
                src_ref=block(*blk) if src is None else src, dst_ref=block(*blk), send_sem=send_sems.at[k],
                recv_sem=recv_sems.at[k], device_id=to, device_id_type=MESH)

        all_ref[pl.ds((4 * x + 2 * y + c) * rows, rows), :] = x_ref[...]
        first = [copy(0, me, sibling, src=x_ref)]
        first += [copy(1 + j, me, (*chip, c), src=x_ref) for j, chip in enumerate(chips)]
        for cp in first:
            cp.start()
        passed = [copy(4 + j, (*chip, c), sibling) for j, chip in enumerate(chips)]
        for j, chip in enumerate(chips):
            copy(1 + j, (*chip, c), me).wait_recv()
            passed[j].start()
        copy(0, sibling, me).wait_recv()
        for j, chip in enumerate(chips):
            copy(4 + j, (*chip, 1 - c), me).wait_recv()
        for cp in first + passed:
            cp.wait_send()
        total = all_ref[0:rows, :]
        for d in range(1, 8):
            total = total + all_ref[d * rows:(d + 1) * rows, :]
        out_ref[...] = total

    vm = pl.BlockSpec(memory_space=pltpu.VMEM)
    return _pcall(
        body, after, name="sum_small", in_specs=[vm], out_specs=vm, out_shape=jax.ShapeDtypeStruct((rows, width), F32),
        scratch_shapes=[pltpu.VMEM((8 * rows, width), F32), pltpu.SemaphoreType.DMA((7,)), pltpu.SemaphoreType.DMA((7,))],
    )(part)


def _row_tile(R, C, itemsize=4, budget=1 << 20):
    for t in (512, 256, 128, 64, 32, 16, 8):
        if R % t == 0 and t * C * itemsize <= budget:
            return t
    return R


def _add_all(g, recv, where, *, name):
    _, R, C = g.shape
    half = R // 2
    t = _row_tile(half, C)
    nb = half // t

    def body(w_ref, g_ref, r_ref, o_ref):
        total = g_ref[0].astype(F32)
        for k in range(7):
            total = total + r_ref[k].astype(F32)
        o_ref[...] = total

    grid_spec = pltpu.PrefetchScalarGridSpec(
        num_scalar_prefetch=1, grid=(nb,),
        in_specs=[pl.BlockSpec((1, t, C), lambda i, wr: (wr[0], wr[1] * nb + i, 0)),
                  pl.BlockSpec((7, t, C), lambda i, wr: (0, i, 0))],
        out_specs=pl.BlockSpec((t, C), lambda i, wr: (i, 0)))
    return _pcall(body, name=name, grid_spec=grid_spec, out_shape=jax.ShapeDtypeStruct((half, C), F32),
                  compiler_params=_params("parallel"))(where, g, recv)


def _adamw(w, g, m, v, *, name):
    R, C = w.shape
    t = _row_tile(R, C)
    c1 = 1.0 - ADAM_B1 ** ADAM_STEP
    c2 = 1.0 - ADAM_B2 ** ADAM_STEP

    def body(w_ref, g_ref, m_ref, v_ref, d_ref, nm_ref, nv_ref):
        gv = g_ref[...]
        mn = ADAM_B1 * m_ref[...] + (1.0 - ADAM_B1) * gv
        vn = ADAM_B2 * v_ref[...] + (1.0 - ADAM_B2) * (gv * gv)
        d_ref[...] = -ADAM_LR * ((mn / c1) / (jnp.sqrt(vn / c2) + ADAM_EPS) + ADAM_WD * w_ref[...])
        nm_ref[...] = mn
        nv_ref[...] = vn

    blk = pl.BlockSpec((t, C), lambda i: (i, 0))
    shp = jax.ShapeDtypeStruct((R, C), F32)
    return _pcall(body, name=name, grid=(R // t,), in_specs=[blk] * 4, out_specs=[blk] * 3, out_shape=[shp] * 3,
                  compiler_params=_params("parallel"))(w, g, m, v)


def _adamw_halves(w, mine, theirs, m, v, core, *, name, after=None):
    R, C = w.shape
    half = R // 2
    t = _row_tile(half, C)
    nbh = half // t
    c1 = 1.0 - ADAM_B1 ** ADAM_STEP
    c2 = 1.0 - ADAM_B2 ** ADAM_STEP

    def body(core_ref, w_ref, a_ref, b_ref, m_ref, v_ref, *rest):
        g_ref, d_ref, nm_ref, nv_ref = rest[-4:]
        gv = jnp.where(pl.program_id(0) // nbh == core_ref[0], a_ref[...], b_ref[...])
        mn = ADAM_B1 * m_ref[...] + (1.0 - ADAM_B1) * gv
        vn = ADAM_B2 * v_ref[...] + (1.0 - ADAM_B2) * (gv * gv)
        g_ref[...] = gv
        d_ref[...] = -ADAM_LR * ((mn / c1) / (jnp.sqrt(vn / c2) + ADAM_EPS) + ADAM_WD * w_ref[...])
        nm_ref[...] = mn
        nv_ref[...] = vn

    blk = pl.BlockSpec((t, C), lambda i, cr: (i, 0))
    hblk = pl.BlockSpec((t, C), lambda i, cr: (i % nbh, 0))
    shp = jax.ShapeDtypeStruct((R, C), F32)
    tied = [] if after is None else [after]
    grid_spec = pltpu.PrefetchScalarGridSpec(num_scalar_prefetch=1, grid=(2 * nbh,),
                                             in_specs=[blk, hblk, hblk, blk, blk] + [ANY_SPEC] * len(tied),
                                             out_specs=[blk] * 4)
    return _pcall(body, name=name, grid_spec=grid_spec, out_shape=[shp] * 4,
                  compiler_params=_params("parallel"))(core, w, mine, theirs, m, v, *tied)


BIG = ("w_in", "w_dil_out", "w_fox_out", "w_out", "w_ffn_in", "w_ffn_down")
SMALL = ("norm_mix_g", "b_fgt", "b_gate", "norm_ffn_g", "norm_final_g")
ORDER = ("norm_mix_g", "w_in", "b_fgt", "b_gate", "w_dil_out", "w_fox_out", "w_out", "norm_ffn_g", "w_ffn_in",
         "w_ffn_down", "norm_final_g")
SMALL_ROWS = {"norm_mix_g": (0, 1), "b_gate": (1, 3), "norm_ffn_g": (3, 4), "norm_final_g": (4, 5), "b_fgt": (5, 6)}


def _columns_to_blocks(full, ncol):
    K = full.shape[0]
    return full.reshape(K, 4, ncol).transpose(1, 0, 2)


def _blocks_to_columns(blocks):
    n, K, ncol = blocks.shape
    return blocks.transpose(1, 0, 2).reshape(K, n * ncol)


def kernel(x, norm_mix_g, w_in, b_fgt, b_gate, w_dil_out, w_fox_out, w_out, norm_ffn_g, w_ffn_in, w_ffn_down, norm_final_g, loss_target, m_norm_mix_g, m_w_in, m_b_fgt, m_b_gate, m_w_dil_out, m_w_fox_out, m_w_out, m_norm_ffn_g, m_w_ffn_in, m_w_ffn_down, m_norm_final_g, v_norm_mix_g, v_w_in, v_b_fgt, v_b_gate, v_w_dil_out, v_w_fox_out, v_w_out, v_norm_ffn_g, v_w_ffn_in, v_w_ffn_down, v_norm_final_g):
    weights = dict(norm_mix_g=norm_mix_g, w_in=w_in, b_fgt=b_fgt, b_gate=b_gate, w_dil_out=w_dil_out,
                   w_fox_out=w_fox_out, w_out=w_out, norm_ffn_g=norm_ffn_g, w_ffn_in=w_ffn_in, w_ffn_down=w_ffn_down,
                   norm_final_g=norm_final_g)
    m_in = dict(norm_mix_g=m_norm_mix_g, w_in=m_w_in, b_fgt=m_b_fgt, b_gate=m_b_gate, w_dil_out=m_w_dil_out,
                w_fox_out=m_w_fox_out, w_out=m_w_out, norm_ffn_g=m_norm_ffn_g, w_ffn_in=m_w_ffn_in,
                w_ffn_down=m_w_ffn_down, norm_final_g=m_norm_final_g)
    v_in = dict(norm_mix_g=v_norm_mix_g, w_in=v_w_in, b_fgt=v_b_fgt, b_gate=v_b_gate, w_dil_out=v_w_dil_out,
                w_fox_out=v_w_fox_out, w_out=v_w_out, norm_ffn_g=v_norm_ffn_g, w_ffn_in=v_w_ffn_in,
                w_ffn_down=v_w_ffn_down, norm_final_g=v_norm_final_g)
    c = lax.axis_index("c")
    chip = 2 * lax.axis_index("x") + lax.axis_index("y")

    shards = {n: weights[n][0].astype(_CD) for n in BIG}
    in_shape = jax.ShapeDtypeStruct((4,) + shards["w_in"].shape, _CD)
    send_i, recv_i, in_src, in_land, token_in = _split_copy_start(
        [shards["w_in"]], [in_shape], _gather_copies, norm_mix_g, name="gather_in_start")
    late = BIG[1:]
    send_g, recv_g, late_src, late_land, token = _split_copy_start(
        [shards[n] for n in late], [jax.ShapeDtypeStruct((4,) + shards[n].shape, _CD) for n in late],
        _gather_whole_copies, token_in, name="gather_late_start")
    adam_in = [t[0] + token_in[0, 0] for t in (w_in, m_w_in, v_w_in)]
    p = dict(norm_mix_g=norm_mix_g, b_fgt=jnp.pad(b_fgt, ((0, 0), (0, F_PAD - N_FOX_HEADS))), b_gate=b_gate,
             norm_ffn_g=norm_ffn_g, norm_final_g=norm_final_g.reshape(1, D_MODEL))

    def first_weights(after):
        own, lands = _split_copy_wait(send_i, recv_i, in_src, in_land, _gather_copies, [after] + adam_in,
                                      name="gather_in_wait")
        (g_in,) = _forward_halves(lands, name="gather_in_forward")
        full_in = _blocks_to_columns(lax.dynamic_update_index_in_dim(g_in, own[0], chip, 0))
        o3 = QKV_COLS
        o4 = o3 + N_FOX_HEADS
        return dict(qkv=full_in[:, :o3], f=jnp.pad(full_in[:, o3:o4], ((0, 0), (0, F_PAD - N_FOX_HEADS))),
                    g=full_in[:, o4:])

    def late_weights(after):
        own, lands = _split_copy_wait(send_g, recv_g, late_src, late_land, _gather_whole_copies, after,
                                      name="gather_late_wait")
        g_dil, g_fox, g_out, g_ffn_in, g_ffn_down = [
            lax.dynamic_update_index_in_dim(l, s, chip, 0) for l, s in zip(lands, own)]
        return dict(dil_out=_blocks_to_columns(g_dil), fox_out=_blocks_to_columns(g_fox),
                    out=g_out.reshape(D_MODEL, D_MODEL), ffn_in=g_ffn_in,
                    ffn_down=g_ffn_down.reshape(D_FF, D_MODEL))

    def to_blocks(n, full):
        shape = weights[n].shape
        if full.ndim == 3:
            return full
        if n in ("w_out", "w_ffn_down"):
            return full.reshape(4, shape[1], shape[2])
        return _columns_to_blocks(full, shape[2])

    in_flight = {}

    def grad_sink(group, gw):
        if group == "in":
            named = {"w_in": jnp.concatenate([gw["qkv"], gw["f"][:, :N_FOX_HEADS], gw["g"]], axis=1)}
        else:
            named = {"w_" + k: v for k, v in gw.items()}
        srcs = [to_blocks(n, named[n]) for n in named]
        lands = [jax.ShapeDtypeStruct((7, s.shape[1] // 2, s.shape[2]), s.dtype) for s in srcs]
        started = _split_copy_start(srcs, lands, _scatter_all_copies, next(iter(gw.values())),
                                    name=f"scatter_{group}_start")
        in_flight[group] = (list(named), started)
        return started[-1]

    loss_part, grad_x, gw, small = _layer_step(x[0], loss_target[0], {}, p, late_weights, grad_sink,
                                               (token_in, token), first_weights)

    where = jnp.stack([chip, c]).astype(jnp.int32)

    def summed_halves(groups, after, name):
        halves = {}
        for group in groups:
            names, (send_s, recv_s, srcs, lands, _) = in_flight[group]
            srcs, recv = _split_copy_wait(send_s, recv_s, srcs, lands, _scatter_all_copies, after,
                                          name=f"scatter_{group}_wait")
            halves.update({n: _add_all(s, r, where, name=f"add_all_{n}") for n, s, r in zip(names, srcs, recv)})
        return {n: (h, o) for (n, h), o in zip(halves.items(), _share_halves(list(halves.values()), name=name))}

    grad_halves = summed_halves(("ffn", "mix"), grad_x, "share_halves")

    out_g, out_d, out_m, out_v = {}, {}, {}, {}
    core = jnp.reshape(c, (1,)).astype(jnp.int32)

    def adamw_big(n, after=None):
        shape = weights[n].shape
        wmv = adam_in if n == "w_in" else [t[0] for t in (weights[n], m_in[n], v_in[n])]
        mine, theirs = grad_halves[n]
        outs = _adamw_halves(wmv[0], mine, theirs, wmv[1], wmv[2], core, name=f"adamw_{n}", after=after)
        out_g[n], out_d[n], out_m[n], out_v[n] = [t.reshape(shape) for t in outs]

    early = ("w_dil_out", "w_fox_out", "w_out", "w_ffn_down")
    for n in early:
        adamw_big(n)
    grad_halves.update(summed_halves(("in",), [grad_x] + [out_d[n] for n in early], "share_halves_in"))
    adamw_big("w_in")

    packed = jnp.concatenate([
        small["norm_mix_g"], small["b_gate"].reshape(2, D_MODEL), small["norm_ffn_g"], small["norm_final_g"],
        jnp.pad(small["b_fgt"], ((0, 0), (0, D_MODEL - F_PAD))), jnp.pad(loss_part, ((0, 0), (0, D_MODEL - 1))),
        jnp.zeros((1, D_MODEL), F32)], axis=0)
    summed = _sum_small(packed, after=out_d["w_in"])
    loss = summed[6, 0]
    adamw_big("w_ffn_in", after=summed)

    for n in SMALL:
        lo, hi = SMALL_ROWS[n]
        shape = weights[n].shape
        g2 = summed[lo:hi].reshape(1, -1)[:, :weights[n].size]
        d2, m2, v2 = _adamw(weights[n].reshape(g2.shape), g2, m_in[n].reshape(g2.shape), v_in[n].reshape(g2.shape),
                            name=f"adamw_{n}")
        out_g[n], out_d[n], out_m[n], out_v[n] = [t.reshape(shape) for t in (g2, d2, m2, v2)]
    return (loss, grad_x[None], *[out_g[n] for n in ORDER], *[out_d[n] for n in ORDER],
            *[out_m[n] for n in ORDER], *[out_v[n] for n in ORDER])
```

```python
import numpy as np
import jax
import jax.numpy as jnp
from jax import lax
from jax.experimental import pallas as pl
from jax.experimental.pallas import tpu as pltpu

F32 = jnp.float32
_CD = jnp.bfloat16

D_MODEL = 1024
HEAD_DIM = 64
DIL_PAIRS = ((128, 1), (512, 4), (2048, 16))
N_DIL_GROUPS = 3
DIL_HEADS = 4
DIL_W = 128
DIL_OUT = DIL_HEADS * HEAD_DIM
DIL_WIDTH = N_DIL_GROUPS * DIL_OUT
N_FOX_HEADS = 8
FOX_WIDTH = N_FOX_HEADS * HEAD_DIM
D_FF = 2816
QKV_COLS = 3 * DIL_WIDTH + 3 * FOX_WIDTH
F_PAD = 128
RMS_EPS = 1e-6
NEG_INF = -1e30
ATTN_SCALE = HEAD_DIM ** -0.5
ADAM_LR, ADAM_B1, ADAM_B2, ADAM_EPS, ADAM_WD, ADAM_STEP = 0.001, 0.9, 0.999, 1e-08, 0.01, 10

VMEM_LIMIT = 48 * 1024 * 1024
VMEM_LIMIT_RESIDENT = 56 * 1024 * 1024
LANES = 128
MESH = pl.DeviceIdType.MESH
HBM_SPEC = pl.BlockSpec(memory_space=pltpu.HBM)


def _pcall(body, after=None, **kw):
    if after is None:
        return pl.pallas_call(body, **kw)
    n_in = len(kw["in_specs"])
    kw["in_specs"] = list(kw["in_specs"]) + [pl.BlockSpec(memory_space=pl.ANY)]

    def tied(*refs):
        return body(*refs[:n_in], *refs[n_in + 1:])

    call = pl.pallas_call(tied, **kw)
    return lambda *args: call(*args, after)


def _params(*sem):
    return pltpu.CompilerParams(dimension_semantics=sem, vmem_limit_bytes=VMEM_LIMIT)


def _pick(dim, pref):
    t = (min(pref, dim) // 128) * 128
    while t >= 128:
        if dim % t == 0:
            return t
        t -= 128
    return dim


def _mm(a, b, *, name, ta=False, tb=False, out_dtype=F32, add=None, tm=1024, tn=512, tk=2048, after=None,
        b_blocks=False, out_blocks=None, a_halves=False, b_halves=False, rms_bwd=None, more=()):
    if a_halves:
        M, K = a.shape[1], 2 * a.shape[2]
    elif ta:
        K, M = a.shape
    else:
        M, K = a.shape
    if b_halves:
        b_rows, b_cols = b.shape[1], 2 * b.shape[2]
    else:
        b_rows, b_cols = (b.shape[1], b.shape[0] * b.shape[2]) if b_blocks else b.shape
    if tb:
        N, K2 = b_rows, b_cols
    else:
        K2, N = b_rows, b_cols
    assert K == K2, (a.shape, b.shape)
    shard = b.shape[2] if b_blocks else None
    tm = _pick(M, tm)
    tn = _pick(shard if (b_blocks and not tb) else (out_blocks or N), tn)
    tk = _pick(shard if (b_blocks and tb) else K, tk)
    nk = K // tk
    dn = (((0 if ta else 1,), (1 if tb else 0,)), ((), ()))
    has_add = add is not None
    assert not (has_add and out_blocks)
    has_norm = rms_bwd is not None
    if has_norm:
        tn = N
        assert not out_blocks and out_dtype == F32
    assert not more or (nk == 1 and tb and not ta)

    def body(*refs):
        a_ref, b_ref = refs[0], refs[1]
        rest = list(refs[2:])
        more_refs = [(rest.pop(0), rest.pop(0)) for _ in more]
        add_ref = rest.pop(0) if has_add else None
        x_ref, g_ref, dres_ref = (rest.pop(0), rest.pop(0), rest.pop(0)) if has_norm else (None, None, None)
        o_ref = rest.pop(0)
        dg_ref = rest.pop(0) if has_norm else None
        bv = b_ref[0] if b_blocks else b_ref[...]
        p = lax.dot_general(a_ref[...].astype(_CD), bv.astype(_CD), dn, preferred_element_type=F32)
        for a2_ref, b2_ref in more_refs:
            p += lax.dot_general(a2_ref[...].astype(_CD), b2_ref[...].astype(_CD), dn, preferred_element_type=F32)

        def finish(r):
            if has_add:
                r = r + add_ref[...]
            if has_norm:
                xv = x_ref[...]
                rs = lax.rsqrt(jnp.mean(xv * xv, axis=-1, keepdims=True) + RMS_EPS)
                xh = xv * rs
                dxh = r * g_ref[...]
                o_ref[...] = dres_ref[...] + rs * (dxh - xh * jnp.mean(dxh * xh, axis=-1, keepdims=True))
                part = jnp.sum(r * xh, axis=0, keepdims=True)
                first = pl.program_id(0) == 0

                @pl.when(first)
                def _():
                    dg_ref[...] = part

                @pl.when(jnp.logical_not(first))
                def _():
                    dg_ref[...] += part
            elif out_blocks:
                o_ref[0] = r.astype(out_dtype)
            else:
                o_ref[...] = r.astype(out_dtype)

        if nk == 1:
            finish(p)
        else:
            acc_ref = rest.pop(0)
            k = pl.program_id(2)

            @pl.when(k == 0)
            def _():
                acc_ref[...] = p

            @pl.when(k > 0)
            def _():
                acc_ref[...] += p

            @pl.when(k == nk - 1)
            def _():
                finish(acc_ref[...])

    if a_halves:
        ka = (K // 2) // tk
        a_spec = pl.BlockSpec((None, tm, tk), lambda i, j, k: (k // ka, i, k % ka))
    else:
        a_spec = pl.BlockSpec((tk, tm), lambda i, j, k: (k, i)) if ta else pl.BlockSpec((tm, tk), lambda i, j, k: (i, k))
    if b_halves:
        nb_ = (N // 2) // tn
        b_spec = pl.BlockSpec((None, tk, tn), lambda i, j, k: (j // nb_, k, j % nb_))
    elif b_blocks and tb:
        per = shard // tk
        b_spec = pl.BlockSpec((1, tn, tk), lambda i, j, k: (k // per, j, k % per))
    elif b_blocks:
        per = shard // tn
        b_spec = pl.BlockSpec((1, tk, tn), lambda i, j, k: (j // per, k, j % per))
    else:
        b_spec = pl.BlockSpec((tn, tk), lambda i, j, k: (j, k)) if tb else pl.BlockSpec((tk, tn), lambda i, j, k: (k, j))
    if out_blocks:
        oper = out_blocks // tn
        o_spec = pl.BlockSpec((1, tm, tn), lambda i, j, k: (j // oper, i, j % oper))
        out_shape = jax.ShapeDtypeStruct((N // out_blocks, M, out_blocks), out_dtype)
    else:
        o_spec = pl.BlockSpec((tm, tn), lambda i, j, k: (i, j))
        out_shape = jax.ShapeDtypeStruct((M, N), out_dtype)
    in_specs, args = [a_spec, b_spec], (a, b)
    for a2, b2 in more:
        assert a2.shape[0] == M and b2.shape == (N, a2.shape[1]), (a2.shape, b2.shape)
        in_specs += [pl.BlockSpec((tm, a2.shape[1]), lambda i, j, k: (i, 0)),
                     pl.BlockSpec((tn, a2.shape[1]), lambda i, j, k: (j, 0))]
        args += (a2, b2)
    if has_add:
        in_specs, args = in_specs + [o_spec], args + (add,)
    out_specs, semantics = o_spec, ("parallel", "parallel", "arbitrary")
    if has_norm:
        vec = pl.BlockSpec((1, N), lambda i, j, k: (0, 0))
        in_specs += [o_spec, vec, o_spec]
        args += tuple(rms_bwd)
        out_specs, out_shape = [o_spec, vec], [out_shape, jax.ShapeDtypeStruct((1, N), F32)]
        semantics = ("arbitrary", "arbitrary", "arbitrary")
    return _pcall(
        body, after, name=name, grid=(M // tm, N // tn, nk), in_specs=in_specs, out_specs=out_specs,
        out_shape=out_shape,
        scratch_shapes=[pltpu.VMEM((tm, tn), F32)] if nk > 1 else [],
        compiler_params=_params(*semantics),
    )(*args)


def _rms_fwd(x, g, *, name, tm=512, after=None):
    S, D = x.shape

    def body(x_ref, g_ref, h_ref):
        xv = x_ref[...]
        r = lax.rsqrt(jnp.mean(xv * xv, axis=-1, keepdims=True) + RMS_EPS)
        h_ref[...] = ((xv * r) * g_ref[...]).astype(h_ref.dtype)

    row = pl.BlockSpec((tm, D), lambda i: (i, 0))
    return _pcall(body, after, name=name, grid=(S // tm,), in_specs=[row, pl.BlockSpec((1, D), lambda i: (0, 0))],
                  out_specs=row, out_shape=jax.ShapeDtypeStruct((S, D), _CD), compiler_params=_params("parallel"))(x, g)


def _ffn_down_loss(act, w_down, x1, g, tgt, *, name, tm=512):
    S, D = x1.shape
    F = act.shape[1]

    def body(a_ref, b_ref, x_ref, g_ref, t_ref, loss_ref, dx_ref, dg_ref):
        xv = x_ref[...] + jnp.dot(a_ref[...].astype(_CD), b_ref[...].astype(_CD), preferred_element_type=F32)
        gv = g_ref[...]
        r = lax.rsqrt(jnp.mean(xv * xv, axis=-1, keepdims=True) + RMS_EPS)
        xh = xv * r
        err = xh * gv - t_ref[...]
        lpart = 0.5 * jnp.sum(jnp.mean(err * err, axis=-1, keepdims=True), axis=0, keepdims=True)
        dy = err * (1.0 / D)
        dxh = dy * gv
        dx_ref[...] = r * (dxh - xh * jnp.mean(dxh * xh, axis=-1, keepdims=True))
        gpart = jnp.sum(dy * xh, axis=0, keepdims=True)

        @pl.when(pl.program_id(0) == 0)
        def _():
            loss_ref[...] = lpart
            dg_ref[...] = gpart

        @pl.when(pl.program_id(0) > 0)
        def _():
            loss_ref[...] += lpart
            dg_ref[...] += gpart

    row = pl.BlockSpec((tm, D), lambda i: (i, 0))
    vec = pl.BlockSpec((1, D), lambda i: (0, 0))
    one = pl.BlockSpec((1, 1), lambda i: (0, 0))
    return _pcall(body, name=name, grid=(S // tm,),
                  in_specs=[pl.BlockSpec((tm, F), lambda i: (i, 0)), pl.BlockSpec((F, D), lambda i: (0, 0)), row, vec, row],
                  out_specs=[one, row, vec],
                  out_shape=[jax.ShapeDtypeStruct((1, 1), F32), jax.ShapeDtypeStruct((S, D), F32),
                             jax.ShapeDtypeStruct((1, D), F32)],
                  compiler_params=_params("arbitrary"))(act, w_down, x1, g, tgt)


def _sigmoid(z):
    return 1.0 / (1.0 + jnp.exp(-z))


def _gated_mix_out(gl, bg, ya, yb, w_out, x, g, *, name, tm=512):
    S, D = ya.shape

    def body(za_ref, zb_ref, ba_ref, bb_ref, ya_ref, yb_ref, w_ref, x_ref, g_ref, m_ref, x1_ref, h_ref):
        ga = _sigmoid(za_ref[...].astype(F32) + ba_ref[...])
        gb = _sigmoid(zb_ref[...].astype(F32) + bb_ref[...])
        merged = (ga * ya_ref[...].astype(F32) + gb * yb_ref[...].astype(F32)).astype(m_ref.dtype)
        m_ref[...] = merged
        x1 = x_ref[...] + jnp.dot(merged, w_ref[...].astype(_CD), preferred_element_type=F32)
        x1_ref[...] = x1
        rs = lax.rsqrt(jnp.mean(x1 * x1, axis=-1, keepdims=True) + RMS_EPS)
        h_ref[...] = ((x1 * rs) * g_ref[...]).astype(h_ref.dtype)

    lo = pl.BlockSpec((tm, D), lambda i: (i, 0))
    hi = pl.BlockSpec((tm, D), lambda i: (i, 1))
    vlo = pl.BlockSpec((1, D), lambda i: (0, 0))
    vhi = pl.BlockSpec((1, D), lambda i: (0, 1))
    whole = pl.BlockSpec((D, D), lambda i: (0, 0))
    return _pcall(body, name=name, grid=(S // tm,), in_specs=[lo, hi, vlo, vhi, lo, lo, whole, lo, vlo],
                  out_specs=[lo, lo, lo],
                  out_shape=[jax.ShapeDtypeStruct((S, D), _CD), jax.ShapeDtypeStruct((S, D), F32),
                             jax.ShapeDtypeStruct((S, D), _CD)],
                  compiler_params=_params("parallel"))(gl, gl, bg, bg, ya, yb, w_out, x, g)


def _gate_bwd(dx1, w_out, gl, bg, ya, yb, *, name, tm=512):
    S, D = ya.shape
    nt = (((1,), (1,)), ((), ()))

    def body(dx_ref, w_ref, za_ref, zb_ref, ba_ref, bb_ref, ya_ref, yb_ref, dya_ref, dyb_ref, dgl_ref, dbg_ref):
        dmv = lax.dot_general(dx_ref[...].astype(_CD), w_ref[...].astype(_CD), nt, preferred_element_type=F32)
        ga = _sigmoid(za_ref[...].astype(F32) + ba_ref[...])
        gb = _sigmoid(zb_ref[...].astype(F32) + bb_ref[...])
        dya_ref[...] = (dmv * ga).astype(dya_ref.dtype)
        dyb_ref[...] = (dmv * gb).astype(dyb_ref.dtype)
        dza = dmv * ya_ref[...].astype(F32) * ga * (1.0 - ga)
        dzb = dmv * yb_ref[...].astype(F32) * gb * (1.0 - gb)
        dgl_ref[:, :D] = dza.astype(dgl_ref.dtype)
        dgl_ref[:, D:] = dzb.astype(dgl_ref.dtype)
        pa = jnp.sum(dza, axis=0, keepdims=True)
        pb = jnp.sum(dzb, axis=0, keepdims=True)

        @pl.when(pl.program_id(0) == 0)
        def _():
            dbg_ref[:, :D] = pa
            dbg_ref[:, D:] = pb

        @pl.when(pl.program_id(0) > 0)
        def _():
            dbg_ref[:, :D] += pa
            dbg_ref[:, D:] += pb

    lo = pl.BlockSpec((tm, D), lambda i: (i, 0))
    hi = pl.BlockSpec((tm, D), lambda i: (i, 1))
    vlo = pl.BlockSpec((1, D), lambda i: (0, 0))
    vhi = pl.BlockSpec((1, D), lambda i: (0, 1))
    wide = pl.BlockSpec((tm, 2 * D), lambda i: (i, 0))
    vwide = pl.BlockSpec((1, 2 * D), lambda i: (0, 0))
    whole = pl.BlockSpec((D, D), lambda i: (0, 0))
    return _pcall(body, name=name, grid=(S // tm,), in_specs=[lo, whole, lo, hi, vlo, vhi, lo, lo],
                  out_specs=[lo, lo, wide, vwide],
                  out_shape=[jax.ShapeDtypeStruct((S, D), _CD), jax.ShapeDtypeStruct((S, D), _CD),
                             jax.ShapeDtypeStruct((S, 2 * D), _CD), jax.ShapeDtypeStruct((1, 2 * D), F32)],
                  compiler_params=_params("arbitrary"))(dx1, w_out, gl, gl, bg, bg, ya, yb)


def _ffn_in_act(h2, w_blocks, *, name, tm=512):
    S, D = h2.shape
    _, _, C = w_blocks.shape

    def body(a_ref, bg_ref, bu_ref, g_ref, u_ref, o_ref):
        av = a_ref[...].astype(_CD)
        gv = jnp.dot(av, bg_ref[0].astype(_CD), preferred_element_type=F32)
        uv = jnp.dot(av, bu_ref[0].astype(_CD), preferred_element_type=F32)
        g_ref[...] = gv.astype(g_ref.dtype)
        u_ref[...] = uv.astype(u_ref.dtype)
        o_ref[...] = (gv * _sigmoid(gv) * uv).astype(o_ref.dtype)

    out = pl.BlockSpec((tm, C), lambda i, j: (i, j))
    shp = jax.ShapeDtypeStruct((S, 2 * C), _CD)
    return _pcall(body, name=name, grid=(S // tm, 2),
                  in_specs=[pl.BlockSpec((tm, D), lambda i, j: (i, 0)), pl.BlockSpec((1, D, C), lambda i, j: (j, 0, 0)),
                            pl.BlockSpec((1, D, C), lambda i, j: (2 + j, 0, 0))],
                  out_specs=[out, out, out], out_shape=[shp, shp, shp],
                  compiler_params=_params("parallel", "arbitrary"))(h2, w_blocks, w_blocks)


def _d_swiglu(dx, w_down, gate, up, *, name, tm=512, tn=1408):
    S, D = dx.shape
    F = w_down.shape[0]
    nt = (((1,), (1,)), ((), ()))

    def body(a_ref, b_ref, g_ref, u_ref, o_ref):
        dv = lax.dot_general(a_ref[...].astype(_CD), b_ref[...].astype(_CD), nt, preferred_element_type=F32)
        gv = g_ref[...].astype(F32)
        sg = _sigmoid(gv)
        o_ref[0] = (dv * u_ref[...].astype(F32) * (sg * (1.0 + gv * (1.0 - sg)))).astype(o_ref.dtype)
        o_ref[1] = (dv * (gv * sg)).astype(o_ref.dtype)

    tile = pl.BlockSpec((tm, tn), lambda i, j: (i, j))
    return _pcall(body, name=name, grid=(S // tm, F // tn),
                  in_specs=[pl.BlockSpec((tm, D), lambda i, j: (i, 0)), pl.BlockSpec((tn, D), lambda i, j: (j, 0)),
                            tile, tile],
                  out_specs=pl.BlockSpec((2, tm, tn), lambda i, j: (0, i, j)),
                  out_shape=jax.ShapeDtypeStruct((2, S, F), _CD),
                  compiler_params=_params("parallel", "arbitrary"))(dx, w_down, gate, up)


def _split3(x):
    hi = x.astype(jnp.bfloat16)
    r1 = x - hi.astype(F32)
    mid = r1.astype(jnp.bfloat16)
    lo = (r1 - mid.astype(F32)).astype(jnp.bfloat16)
    return hi, mid, lo


def _ones_dot_left(ones, x):
    return sum(jnp.dot(ones, p, preferred_element_type=F32) for p in _split3(x))


def _ones_dot_right(x, ones):
    return sum(jnp.dot(p, ones, preferred_element_type=F32) for p in _split3(x))


def _head_sum(x):
    n = x.shape[1]
    r = lax.broadcasted_iota(jnp.int32, (n, n), 0) // HEAD_DIM
    c = lax.broadcasted_iota(jnp.int32, (n, n), 1) // HEAD_DIM
    return _ones_dot_right(x, (r == c).astype(jnp.bfloat16))


def _log_sigmoid(z):
    e = jnp.exp(-jnp.abs(z))
    t = 1.0 + e
    log1p_e = jnp.where(t == 1.0, e, jnp.log(t) * (e / jnp.where(t == 1.0, 1.0, t - 1.0)))
    return jnp.minimum(z, 0.0) - log1p_e


def _fox_cumsum(zf, bf, *, name):
    S, W = zf.shape
    nb = S // 128

    def body(z_ref, b_ref, c_ref):
        tri = (lax.broadcasted_iota(jnp.int32, (128, 128), 0) >= lax.broadcasted_iota(jnp.int32, (128, 128), 1))
        tri = tri.astype(jnp.bfloat16)

        def step(i, carry):
            rows = pl.ds(pl.multiple_of(i * 128, 128), 128)
            lf = _log_sigmoid(z_ref[rows, :] + b_ref[...])
            cb = _ones_dot_left(tri, lf) + carry
            c_ref[rows, :] = cb
            return cb[127:128, :]

        lax.fori_loop(0, nb, step, jnp.zeros((1, W), F32))

    return _pcall(body, name=name, out_shape=jax.ShapeDtypeStruct((S, W), F32),
                  compiler_params=pltpu.CompilerParams(vmem_limit_bytes=VMEM_LIMIT))(zf, bf)


def _fox_cumsum_bwd(dc, zf, bf, *, name):
    S, W = zf.shape
    nb = S // 128

    def body(dc_ref, z_ref, b_ref, dz_ref, db_ref):
        tri = (lax.broadcasted_iota(jnp.int32, (128, 128), 0) <= lax.broadcasted_iota(jnp.int32, (128, 128), 1))
        tri = tri.astype(jnp.bfloat16)

        def step(k, carry):
            tail, acc = carry
            i = nb - 1 - k
            rows = pl.ds(pl.multiple_of(i * 128, 128), 128)
            dlf = _ones_dot_left(tri, dc_ref[rows, :]) + tail
            dz = dlf * _sigmoid(-(z_ref[rows, :] + b_ref[...]))
            dz_ref[rows, :] = dz
            return dlf[0:1, :], acc + jnp.sum(dz, axis=0, keepdims=True)

        _, acc = lax.fori_loop(0, nb, step, (jnp.zeros((1, W), F32), jnp.zeros((1, W), F32)))
        db_ref[...] = acc

    return _pcall(body, name=name,
                  out_shape=[jax.ShapeDtypeStruct((S, W), F32), jax.ShapeDtypeStruct((1, W), F32)],
                  compiler_params=pltpu.CompilerParams(vmem_limit_bytes=VMEM_LIMIT))(dc, zf, bf)


def _proj_dil(h, w_qkv, *, name, tm=1024):
    S, D = h.shape
    tn = DIL_WIDTH

    def body(a_ref, b_ref, *rest):
        outs, acc = rest[:N_DIL_GROUPS], rest[N_DIL_GROUPS]
        prod = jnp.dot(a_ref[...].astype(_CD), b_ref[...].astype(_CD), preferred_element_type=F32)
        for k in range(tn // LANES):
            acc[k] = prod[:, k * LANES:(k + 1) * LANES]
        for g, (_, d) in enumerate(DIL_PAIRS):
            for half in range(DIL_OUT // LANES):
                k = g * (DIL_OUT // LANES) + half
                cols = slice(half * LANES, (half + 1) * LANES)
                for r in range(d):
                    rows = pl.ds(r, tm // d, stride=d) if d > 1 else slice(None)
                    outs[g][0, r, :, cols] = acc[k, rows, :].astype(outs[g].dtype)

    out_specs = [pl.BlockSpec((1, d, tm // d, DIL_OUT), lambda i, j: (j, 0, i, 0)) for _, d in DIL_PAIRS]
    out_shape = [jax.ShapeDtypeStruct((3, d, S // d, DIL_OUT), _CD) for _, d in DIL_PAIRS]
    outs = _pcall(body, name=name, grid=(S // tm, 3),
                  in_specs=[pl.BlockSpec((tm, D), lambda i, j: (i, 0)), pl.BlockSpec((D, tn), lambda i, j: (0, j))],
                  out_specs=out_specs, out_shape=out_shape, scratch_shapes=[pltpu.VMEM((tn // LANES, tm, LANES), F32)],
                  compiler_params=_params("parallel", "arbitrary"))(h, w_qkv)
    return [o.reshape(3, S, DIL_OUT) for o in outs]


def _dil_start(block, S, dilation):
    sub = S // dilation
    u0 = block * DIL_W
    return (u0 % sub) * dilation + u0 // sub


def _dil_slopes(group):
    h = np.arange(1, N_DIL_GROUPS * DIL_HEADS + 1, dtype=np.float32)
    s = (np.float32(2.0) ** (np.float32(-8.0) * h / np.float32(N_DIL_GROUPS * DIL_HEADS))).astype(np.float32)
    return [float(v) for v in s.reshape(N_DIL_GROUPS, DIL_HEADS)[group]]


def _dil_tiles(i, n, blocks_per_seq):
    qi = lax.broadcasted_iota(jnp.int32, (DIL_W, 2 * DIL_W), 0)
    kj = lax.broadcasted_iota(jnp.int32, (DIL_W, 2 * DIL_W), 1)
    rel = qi + DIL_W - kj
    first = ((4 * n + i) % blocks_per_seq) == 0
    valid = jnp.logical_and(jnp.logical_and(rel >= 0, rel <= DIL_W), jnp.logical_or(kj >= DIL_W, jnp.logical_not(first)))
    return valid, rel.astype(F32)


def _dil_window(cur_ref, prev_ref, i, cols):
    if i > 0:
        return cur_ref[(i - 1) * DIL_W:(i + 1) * DIL_W, cols]
    return jnp.concatenate([prev_ref[:, cols], cur_ref[:DIL_W, cols]], axis=0)


CHUNK = 4 * DIL_W


def _dil_rows(block, S, dilation):
    start = _dil_start(block, S, dilation)
    return pl.ds(start, DIL_W, stride=dilation) if dilation > 1 else pl.ds(start, DIL_W)


def SPLIT(S):
    return (DIL_OUT // LANES, S, LANES)


def _dil_fwd(qkv, group, *, name):
    S = qkv.shape[1]
    dilation = DIL_PAIRS[group][1]
    bps = (S // dilation) // DIL_W
    slopes = _dil_slopes(group)
    nt = (((1,), (1,)), ((), ()))

    def body(q_ref, k_ref, v_ref, kp_ref, vp_ref, on_ref, ln_ref, o_ref, l_ref):
        n = pl.program_id(0)
        for i in range(4):
            valid, rel = _dil_tiles(i, n, bps)
            rows = slice(i * DIL_W, (i + 1) * DIL_W)
            for h in range(DIL_HEADS):
                cols = slice(h * HEAD_DIM, (h + 1) * HEAD_DIM)
                qh = q_ref[rows, cols]
                k2, v2 = _dil_window(k_ref, kp_ref, i, cols), _dil_window(v_ref, vp_ref, i, cols)
                s = lax.dot_general(qh, k2, nt, preferred_element_type=F32) * ATTN_SCALE - (slopes[h] * dilation) * rel
                s = jnp.where(valid, s, NEG_INF)
                m = jnp.max(s, axis=-1, keepdims=True)
                p = jnp.exp(s - m)
                den = jnp.sum(p, axis=-1, keepdims=True)
                acc = jnp.dot(p.astype(_CD), v2, preferred_element_type=F32)
                o_ref[rows, cols] = acc / den
                l_ref[rows, cols] = jnp.broadcast_to(m + jnp.log(den), (DIL_W, HEAD_DIM))
        for i in range(4):
            rows = slice(i * DIL_W, (i + 1) * DIL_W)
            nat = _dil_rows(4 * n + i, S, dilation)
            for half in range(DIL_OUT // LANES):
                cols = slice(half * LANES, (half + 1) * LANES)
                on_ref[half, nat, :] = o_ref[rows, cols]
                ln_ref[half, nat, :] = l_ref[rows, cols]

    def cur(which):
        return pl.BlockSpec((None, CHUNK, DIL_OUT), lambda n: (which, n, 0))

    def prev(which):
        return pl.BlockSpec((None, DIL_W, DIL_OUT), lambda n: (which, jnp.maximum(4 * n - 1, 0), 0))

    whole = pl.BlockSpec(SPLIT(S), lambda n: (0, 0, 0))
    return _pcall(body, name=name, grid=(S // CHUNK,), in_specs=[cur(0), cur(1), cur(2), prev(1), prev(2)],
                  out_specs=[whole, whole],
                  out_shape=[jax.ShapeDtypeStruct(SPLIT(S), F32), jax.ShapeDtypeStruct(SPLIT(S), F32)],
                  scratch_shapes=[pltpu.VMEM((CHUNK, DIL_OUT), F32), pltpu.VMEM((CHUNK, DIL_OUT), F32)],
                  compiler_params=_params("arbitrary"))(qkv, qkv, qkv, qkv, qkv)


STAT_OFFSET = HEAD_DIM // 2


def _dil_bwd(qkv, stats, do, group, *, name):
    S = qkv.shape[1]
    dilation = DIL_PAIRS[group][1]
    bps = (S // dilation) // DIL_W
    slopes = _dil_slopes(group)
    nchunk = S // CHUNK
    nt = (((1,), (1,)), ((), ()))
    tn = (((0,), (0,)), ((), ()))

    def body(q_ref, k_ref, v_ref, kp_ref, vp_ref, ln_ref, don_ref, dqn_ref, dkn_ref, dvn_ref,
             dk_s, dv_s, l_ref, do_ref, dq_ref):
        step = pl.program_id(0)
        n = nchunk - 1 - step
        for i in range(4):
            rows = slice(i * DIL_W, (i + 1) * DIL_W)
            nat = _dil_rows(4 * n + i, S, dilation)
            for half in range(DIL_OUT // LANES):
                cols = slice(half * LANES, (half + 1) * LANES)
                l_ref[rows, cols] = ln_ref[half, nat, :]
                do_ref[rows, cols] = don_ref[half, nat, :]

        @pl.when(step == 0)
        def _():
            dk_s[:, CHUNK:] = jnp.zeros((DIL_OUT, DIL_W), F32)
            dv_s[:, CHUNK:] = jnp.zeros((DIL_OUT, DIL_W), F32)

        dk_s[:, :CHUNK] = jnp.zeros((DIL_OUT, CHUNK), F32)
        dv_s[:, :CHUNK] = jnp.zeros((DIL_OUT, CHUNK), F32)
        for i in range(4):
            valid, rel = _dil_tiles(i, n, bps)
            rows = slice(i * DIL_W, (i + 1) * DIL_W)
            window = slice(i * DIL_W, (i + 2) * DIL_W)
            for h in range(DIL_HEADS):
                cols = slice(h * HEAD_DIM, (h + 1) * HEAD_DIM)
                qh = q_ref[rows, cols]
                k2, v2 = _dil_window(k_ref, kp_ref, i, cols), _dil_window(v_ref, vp_ref, i, cols)
                lh = l_ref[rows, h * HEAD_DIM:h * HEAD_DIM + 1]
                shift = l_ref[rows, h * HEAD_DIM + STAT_OFFSET:h * HEAD_DIM + STAT_OFFSET + 1]
                s = lax.dot_general(qh, k2, nt, preferred_element_type=F32) * ATTN_SCALE - (slopes[h] * dilation) * rel
                p = jnp.exp(jnp.where(valid, s, NEG_INF) - lh)
                dob = do_ref[rows, cols].astype(_CD)
                ds = p * (lax.dot_general(dob, v2, nt, preferred_element_type=F32) + shift)
                dsb = (ds * ATTN_SCALE).astype(_CD)
                dq_ref[rows, cols] = jnp.dot(dsb, k2, preferred_element_type=F32)
                dk_s[cols, window] += lax.dot_general(qh, dsb, tn, preferred_element_type=F32)
                dv_s[cols, window] += lax.dot_general(dob, p.astype(_CD), tn, preferred_element_type=F32)
        for i in range(4):
            rows = slice(i * DIL_W, (i + 1) * DIL_W)
            done = slice((i + 1) * DIL_W, (i + 2) * DIL_W)
            nat = _dil_rows(4 * n + i, S, dilation)
            dkb, dvb = dk_s[:, done].T, dv_s[:, done].T
            for half in range(DIL_OUT // LANES):
                cols = slice(half * LANES, (half + 1) * LANES)
                dqn_ref[half, nat, :] = dq_ref[rows, cols]
                dkn_ref[half, nat, :] = dkb[:, cols]
                dvn_ref[half, nat, :] = dvb[:, cols]
        dk_s[:, CHUNK:] = dk_s[:, :DIL_W]
        dv_s[:, CHUNK:] = dv_s[:, :DIL_W]

    def cur(which):
        return pl.BlockSpec((None, CHUNK, DIL_OUT), lambda s: (which, nchunk - 1 - s, 0))

    def prev(which):
        return pl.BlockSpec((None, DIL_W, DIL_OUT), lambda s: (which, jnp.maximum(4 * (nchunk - 1 - s) - 1, 0), 0))

    whole = pl.BlockSpec(SPLIT(S), lambda s: (0, 0, 0))
    shp = jax.ShapeDtypeStruct(SPLIT(S), F32)
    tile = pltpu.VMEM((CHUNK, DIL_OUT), F32)
    return _pcall(body, name=name, grid=(nchunk,),
                  in_specs=[cur(0), cur(1), cur(2), prev(1), prev(2), whole, whole],
                  out_specs=[whole, whole, whole], out_shape=[shp, shp, shp],
                  scratch_shapes=[pltpu.VMEM((DIL_OUT, CHUNK + DIL_W), F32), pltpu.VMEM((DIL_OUT, CHUNK + DIL_W), F32),
                                  tile, tile, tile],
                  compiler_params=pltpu.CompilerParams(dimension_semantics=("arbitrary",),
                                                       vmem_limit_bytes=VMEM_LIMIT_RESIDENT))(
        qkv, qkv, qkv, qkv, qkv, stats, do)


def _dil_mix_fwd(os_, ls_, *, name, tm=512):
    nh, S, _ = os_[0].shape

    def body(o0, o1, o2, l0, l1, l2, out_ref):
        for half in range(nh):
            ls = [l0[half], l1[half], l2[half]]
            m = jnp.maximum(jnp.maximum(ls[0], ls[1]), ls[2])
            es = [jnp.exp(l - m) for l in ls]
            den = es[0] + es[1] + es[2]
            mixed = (es[0] * o0[half] + es[1] * o1[half] + es[2] * o2[half]) / den
            out_ref[:, half * LANES:(half + 1) * LANES] = mixed.astype(out_ref.dtype)

    halves = pl.BlockSpec((nh, tm, LANES), lambda i: (0, i, 0))
    row = pl.BlockSpec((tm, nh * LANES), lambda i: (i, 0))
    return _pcall(body, name=name, grid=(S // tm,), in_specs=[halves] * 6, out_specs=row,
                  out_shape=jax.ShapeDtypeStruct((S, nh * LANES), _CD), compiler_params=_params("parallel"))(*os_, *ls_)


def _dil_mix_bwd(doa, os_, ls_, *, name, tm=512, after=None):
    nh, S, _ = os_[0].shape

    def body(d_ref, o0, o1, o2, l0, l1, l2, do0, do1, do2, st0, st1, st2):
        first = lax.broadcasted_iota(jnp.int32, (tm, LANES), 1) % HEAD_DIM < STAT_OFFSET
        for half in range(nh):
            dv = d_ref[:, half * LANES:(half + 1) * LANES]
            ls = [l0[half], l1[half], l2[half]]
            m = jnp.maximum(jnp.maximum(ls[0], ls[1]), ls[2])
            es = [jnp.exp(l - m) for l in ls]
            den = es[0] + es[1] + es[2]
            al = [e / den for e in es]
            da = [_head_sum(dv * o[half]) for o in (o0, o1, o2)]
            mean = al[0] * da[0] + al[1] * da[1] + al[2] * da[2]
            for a, l, do_ref, st_ref in zip(al, ls, (do0, do1, do2), (st0, st1, st2)):
                do_ref[half] = a * dv
                st_ref[half] = jnp.where(first, l, -a * mean)

    halves = pl.BlockSpec((nh, tm, LANES), lambda i: (0, i, 0))
    row = pl.BlockSpec((tm, nh * LANES), lambda i: (i, 0))
    shp = jax.ShapeDtypeStruct((nh, S, LANES), F32)
    return _pcall(body, after, name=name, grid=(S // tm,), in_specs=[row] + [halves] * 6, out_specs=[halves] * 6,
                  out_shape=[shp] * 6, compiler_params=_params("parallel"))(doa, *os_, *ls_)


FOX_T = 512


PACK = 2 * HEAD_DIM
HEAD_PAIRS = N_FOX_HEADS // 2
FOX_HPS = 8
Q_BLOCK0 = 0
K_BLOCK0 = FOX_WIDTH // PACK
V_BLOCK0 = 2 * FOX_WIDTH // PACK


def _pieces(x):
    hi = x.astype(jnp.bfloat16).astype(F32)
    r = x - hi
    mid = r.astype(jnp.bfloat16).astype(F32)
    lo = (r - mid).astype(jnp.bfloat16).astype(F32)
    return [hi, mid, lo]


def _extras(first, second, rows):
    lane = lax.broadcasted_iota(jnp.int32, (rows, HEAD_DIM), 1)
    out = jnp.zeros((rows, HEAD_DIM), F32)
    for base, triple in ((0, first), (3, second)):
        if all(isinstance(v, float) for v in triple) and len(set(triple)) == 1:
            if triple[0] != 0.0:
                out = jnp.where(jnp.logical_and(lane >= base, lane < base + 3), triple[0], out)
        else:
            for idx, val in enumerate(triple):
                out = jnp.where(lane == base + idx, val, out)
    return out


def _head_column(c, h):
    lane = lax.broadcasted_iota(jnp.int32, c.shape, 1)
    return jnp.sum(jnp.where(lane == h, c, 0.0), axis=1, keepdims=True)


ONES3 = [1.0, 1.0, 1.0]
ZEROS3 = [0.0, 0.0, 0.0]


def _fox_pack_fwd(qkv, c, *, name, tm=1024):
    S = qkv.shape[0]

    def body(q_ref, k_ref, v_ref, c_ref, qo_ref, ko_ref, vo_ref):
        hp = pl.program_id(1)
        cv = c_ref[...]
        v_extras = jnp.where(lax.broadcasted_iota(jnp.int32, (tm, HEAD_DIM), 1) < 3, 1.0, 0.0).astype(vo_ref.dtype)
        for hh in range(2):
            ch = _pieces(_head_column(cv, 2 * hp + hh))
            src = slice(hh * HEAD_DIM, (hh + 1) * HEAD_DIM)
            lo = slice(hh * PACK, hh * PACK + HEAD_DIM)
            hi = slice(hh * PACK + HEAD_DIM, (hh + 1) * PACK)
            qo_ref[:, lo] = (q_ref[:, src].astype(F32) * ATTN_SCALE).astype(qo_ref.dtype)
            qo_ref[:, hi] = _extras(ch, ONES3, tm).astype(qo_ref.dtype)
            ko_ref[:, lo] = k_ref[:, src]
            ko_ref[:, hi] = _extras(ONES3, [-p for p in ch], tm).astype(ko_ref.dtype)
            vo_ref[:, lo] = v_ref[:, src]
            vo_ref[:, hi] = v_extras

    def src(block0):
        return pl.BlockSpec((tm, PACK), lambda i, hp: (i, block0 + hp))

    out = pl.BlockSpec((tm, 2 * PACK), lambda i, hp: (i, hp))
    shp = jax.ShapeDtypeStruct((S, N_FOX_HEADS * PACK), _CD)
    return _pcall(body, name=name, grid=(S // tm, HEAD_PAIRS),
                  in_specs=[src(Q_BLOCK0), src(K_BLOCK0), src(V_BLOCK0), pl.BlockSpec((tm, PACK), lambda i, hp: (i, 0))],
                  out_specs=[out, out, out], out_shape=[shp, shp, shp],
                  compiler_params=_params("parallel", "parallel"))(qkv, qkv, qkv, c)


def _fox_fwd(qp, kp, vp, *, name):
    S = qp.shape[0]
    nt = S // FOX_T
    nt_dims = (((1,), (1,)), ((), ()))
    tn_dims = (((0,), (0,)), ((), ()))

    def body(i_tab, j_tab, q_ref, k_ref, v_ref, o_ref, l_ref, m_s, acc_s):
        t = pl.program_id(1)
        i, j = i_tab[t], j_tab[t]

        @pl.when(j == 0)
        def _():
            m_s[...] = jnp.full((FOX_HPS, 1, FOX_T), NEG_INF, F32)
            acc_s[...] = jnp.zeros((FOX_HPS, PACK, FOX_T), F32)

        def tile(diagonal):
            for hh in range(FOX_HPS):
                cols = slice(hh * PACK, (hh + 1) * PACK)
                st = lax.dot_general(k_ref[:, cols], q_ref[:, cols], nt_dims, preferred_element_type=F32)
                if diagonal:
                    key = lax.broadcasted_iota(jnp.int32, (FOX_T, FOX_T), 0)
                    qry = lax.broadcasted_iota(jnp.int32, (FOX_T, FOX_T), 1)
                    st = jnp.where(key <= qry, st, NEG_INF)
                m_old = m_s[hh]
                m_new = jnp.maximum(m_old, jnp.max(st, axis=0, keepdims=True))
                pt = jnp.exp(st - m_new)
                acc_s[hh] = jnp.exp(m_old - m_new) * acc_s[hh] + lax.dot_general(
                    v_ref[:, cols], pt.astype(_CD), tn_dims, preferred_element_type=F32)
                m_s[hh] = m_new

        @pl.when(j < i)
        def _():
            tile(False)

        @pl.when(j == i)
        def _():
            tile(True)
            for hh in range(FOX_HPS):
                acc = acc_s[hh]
                den = acc[HEAD_DIM:HEAD_DIM + 1, :]
                cols = slice(hh * HEAD_DIM, (hh + 1) * HEAD_DIM)
                o_ref[:, cols] = (acc[:HEAD_DIM, :] / den).T
                l_ref[:, cols] = jnp.broadcast_to(m_s[hh] + jnp.log(den), (HEAD_DIM, FOX_T)).T

    pairs = [(i, j) for i in range(nt) for j in range(i + 1)]
    i_tab = jnp.asarray([p[0] for p in pairs], jnp.int32)
    j_tab = jnp.asarray([p[1] for p in pairs], jnp.int32)
    qs = pl.BlockSpec((FOX_T, FOX_HPS * PACK), lambda hp, t, it, jt: (it[t], hp))
    ks = pl.BlockSpec((FOX_T, FOX_HPS * PACK), lambda hp, t, it, jt: (jt[t], hp))
    os_ = pl.BlockSpec((FOX_T, FOX_HPS * HEAD_DIM), lambda hp, t, it, jt: (it[t], hp))
    shp = jax.ShapeDtypeStruct((S, FOX_WIDTH), F32)
    grid_spec = pltpu.PrefetchScalarGridSpec(
        num_scalar_prefetch=2, grid=(N_FOX_HEADS // FOX_HPS, len(pairs)), in_specs=[qs, ks, ks], out_specs=[os_, os_],
        scratch_shapes=[pltpu.VMEM((FOX_HPS, 1, FOX_T), F32), pltpu.VMEM((FOX_HPS, PACK, FOX_T), F32)])
    return _pcall(body, name=name, grid_spec=grid_spec, out_shape=[shp, shp],
                  compiler_params=_params("parallel", "arbitrary"))(i_tab, j_tab, qp, kp, vp)


def _fox_pack_bwd(qkv, c, o, lse, do, *, name, tm=1024, after=None):
    S = qkv.shape[0]

    def body(q_ref, c_ref, o_ref, l_ref, do_ref, qo_ref, do_out_ref):
        hp = pl.program_id(1)
        cv = c_ref[...]
        for hh in range(2):
            src = slice(hh * HEAD_DIM, (hh + 1) * HEAD_DIM)
            lo = slice(hh * PACK, hh * PACK + HEAD_DIM)
            hi = slice(hh * PACK + HEAD_DIM, (hh + 1) * PACK)
            shift = _head_column(cv, 2 * hp + hh) - l_ref[:, hh * HEAD_DIM:hh * HEAD_DIM + 1]
            dov = do_ref[:, src]
            dsum = jnp.sum(dov * o_ref[:, src], axis=-1, keepdims=True)
            qo_ref[:, lo] = (q_ref[:, src].astype(F32) * ATTN_SCALE).astype(qo_ref.dtype)
            qo_ref[:, hi] = _extras(_pieces(shift), ONES3, tm).astype(qo_ref.dtype)
            do_out_ref[:, lo] = dov.astype(do_out_ref.dtype)
            do_out_ref[:, hi] = _extras(_pieces(-dsum), ZEROS3, tm).astype(do_out_ref.dtype)

    pair = pl.BlockSpec((tm, PACK), lambda i, hp: (i, hp))
    out = pl.BlockSpec((tm, 2 * PACK), lambda i, hp: (i, hp))
    shp = jax.ShapeDtypeStruct((S, N_FOX_HEADS * PACK), _CD)
    return _pcall(body, after, name=name, grid=(S // tm, HEAD_PAIRS),
                  in_specs=[pl.BlockSpec((tm, PACK), lambda i, hp: (i, Q_BLOCK0 + hp)),
                            pl.BlockSpec((tm, PACK), lambda i, hp: (i, 0)), pair, pair, pair],
                  out_specs=[out, out], out_shape=[shp, shp],
                  compiler_params=_params("parallel", "parallel"))(qkv, c, o, lse, do)


def _fox_bwd(qp, kp, vp, dop, *, name):
    S = qp.shape[0]
    nt = S // FOX_T
    nt_dims = (((1,), (1,)), ((), ()))
    tn_dims = (((0,), (0,)), ((), ()))

    def body(i_tab, j_tab, q_ref, k_ref, v_ref, do_ref, dq_ref, dk_ref, dv_ref, dc_ref, dr_ref,
             dq_s, dk_s, dv_s, dc_s, dr_s):
        t = pl.program_id(1)
        i, j = i_tab[t], j_tab[t]

        @pl.when(t == 0)
        def _():
            dq_s[...] = jnp.zeros((S, FOX_HPS * PACK), F32)
            dr_s[...] = jnp.zeros((FOX_HPS, 1, S), F32)

        @pl.when(i == j)
        def _():
            dk_s[...] = jnp.zeros((FOX_T, FOX_HPS * PACK), F32)
            dv_s[...] = jnp.zeros((FOX_T, FOX_HPS * PACK), F32)
            dc_s[...] = jnp.zeros((FOX_HPS, FOX_T, 1), F32)

        def tile(diagonal):
            rows = pl.ds(pl.multiple_of(i * FOX_T, FOX_T), FOX_T)
            for hh in range(FOX_HPS):
                cols = slice(hh * PACK, (hh + 1) * PACK)
                qv, kv, vv, dov = q_ref[:, cols], k_ref[:, cols], v_ref[:, cols], do_ref[:, cols]
                pt = jnp.exp(lax.dot_general(kv, qv, nt_dims, preferred_element_type=F32))
                if diagonal:
                    key = lax.broadcasted_iota(jnp.int32, (FOX_T, FOX_T), 0)
                    qry = lax.broadcasted_iota(jnp.int32, (FOX_T, FOX_T), 1)
                    pt = jnp.where(key <= qry, pt, 0.0)
                dst = pt * lax.dot_general(vv, dov, nt_dims, preferred_element_type=F32)
                dsb = dst.astype(_CD)
                dc_s[hh] += jnp.sum(dst, axis=1, keepdims=True)
                dr_s[hh, :, rows] += jnp.sum(dst, axis=0, keepdims=True)
                dv_s[:, cols] += jnp.dot(pt.astype(_CD), dov, preferred_element_type=F32)
                dk_s[:, cols] += jnp.dot(dsb, qv, preferred_element_type=F32)
                dq_s[rows, cols] += lax.dot_general(dsb, kv, tn_dims, preferred_element_type=F32)

        @pl.when(i > j)
        def _():
            tile(False)

        @pl.when(i == j)
        def _():
            tile(True)

        @pl.when(i == nt - 1)
        def _():
            for hh in range(FOX_HPS):
                src = slice(hh * PACK, hh * PACK + HEAD_DIM)
                dst_cols = slice(hh * HEAD_DIM, (hh + 1) * HEAD_DIM)
                dk_ref[:, dst_cols] = dk_s[:, src].astype(dk_ref.dtype)
                dv_ref[:, dst_cols] = dv_s[:, src].astype(dv_ref.dtype)
                dc_ref[:, dst_cols] = jnp.broadcast_to(dc_s[hh], (FOX_T, HEAD_DIM))

        @pl.when(t == len(pairs) - 1)
        def _():
            for hh in range(FOX_HPS):
                dq_ref[:, hh * HEAD_DIM:(hh + 1) * HEAD_DIM] = (
                    dq_s[:, hh * PACK:hh * PACK + HEAD_DIM] * ATTN_SCALE).astype(dq_ref.dtype)
            dr_ref[...] = dr_s[...]

    pairs = [(i, j) for j in range(nt) for i in range(j, nt)]
    i_tab = jnp.asarray([p[0] for p in pairs], jnp.int32)
    j_tab = jnp.asarray([p[1] for p in pairs], jnp.int32)
    wide, narrow = FOX_HPS * PACK, FOX_HPS * HEAD_DIM
    qs = pl.BlockSpec((FOX_T, wide), lambda hp, t, it, jt: (it[t], hp))
    ks = pl.BlockSpec((FOX_T, wide), lambda hp, t, it, jt: (jt[t], hp))
    whole = pl.BlockSpec((S, narrow), lambda hp, t, it, jt: (0, hp))
    cs = pl.BlockSpec((FOX_T, narrow), lambda hp, t, it, jt: (jt[t], hp))
    rs = pl.BlockSpec((FOX_HPS, 1, S), lambda hp, t, it, jt: (hp, 0, 0))
    shp = jax.ShapeDtypeStruct((S, FOX_WIDTH), _CD)
    grid_spec = pltpu.PrefetchScalarGridSpec(
        num_scalar_prefetch=2, grid=(N_FOX_HEADS // FOX_HPS, len(pairs)), in_specs=[qs, ks, ks, qs],
        out_specs=[whole, cs, cs, cs, rs],
        scratch_shapes=[pltpu.VMEM((S, wide), F32), pltpu.VMEM((FOX_T, wide), F32),
                        pltpu.VMEM((FOX_T, wide), F32), pltpu.VMEM((FOX_HPS, FOX_T, 1), F32),
                        pltpu.VMEM((FOX_HPS, 1, S), F32)])
    return _pcall(body, name=name, grid_spec=grid_spec,
                  out_shape=[shp, shp, shp, jax.ShapeDtypeStruct((S, FOX_WIDTH), F32),
                             jax.ShapeDtypeStruct((N_FOX_HEADS, 1, S), F32)],
                  compiler_params=_params("parallel", "arbitrary"))(i_tab, j_tab, qp, kp, vp, dop)


def _layer_step(x, tgt, w, p, late_weights=None, grad_sink=None, after=None, first_weights=None):
    S = x.shape[0]
    after_norm, after_proj = after if after is not None else (None, None)
    h = _rms_fwd(x, p["norm_mix_g"], name="rms_mix", after=after_norm)
    if first_weights is not None:
        w = {**w, **first_weights(h)}
    qkv = _mm(h, w["qkv"][:, 3 * DIL_WIDTH:], name="proj_fox", out_dtype=_CD, tn=768, tm=2048, after=after_proj)
    dil_qkv = _proj_dil(h, w["qkv"], name="proj_dil")
    zf = _mm(h, w["f"], name="proj_f")
    gl = _mm(h, w["g"], name="proj_gate", tn=1024, out_dtype=_CD)

    dil_o, dil_l = [], []
    for g in range(N_DIL_GROUPS):
        og, lg = _dil_fwd(dil_qkv[g], g, name=f"dil_fwd{g}")
        dil_o.append(og), dil_l.append(lg)
    o_a = _dil_mix_fwd(dil_o, dil_l, name="dil_mix")

    c = _fox_cumsum(zf, p["b_fgt"], name="fox_cumsum")
    fqp, fkp, fvp = _fox_pack_fwd(qkv, c, name="fox_pack")
    o_b, flse = _fox_fwd(fqp, fkp, fvp, name="fox_fwd")

    if late_weights is not None:
        w = {**w, **late_weights(o_b)}
    y_a = _mm(o_a, w["dil_out"], name="y_a", tn=1024, out_dtype=_CD)
    y_b = _mm(o_b, w["fox_out"], name="y_b", tn=1024, out_dtype=_CD)
    merged, x1, h2 = _gated_mix_out(gl, p["b_gate"], y_a, y_b, w["out"], x, p["norm_ffn_g"], name="mix_out")
    gate, up, act = _ffn_in_act(h2, w["ffn_in"], name="ffn_in")
    loss, dx2, dg_final = _ffn_down_loss(act, w["ffn_down"], x1, p["norm_final_g"], tgt, name="ffn_down_loss")

    gw_ffn_down = _mm(act, dx2, name="gw_ffn_down", ta=True, out_dtype=_CD, tm=1408)
    dgu = _d_swiglu(dx2, w["ffn_down"], gate, up, name="d_swiglu")
    gw_ffn_in = _mm(h2, dgu, name="gw_ffn_in", ta=True, out_dtype=_CD, tn=1408, out_blocks=1408, b_halves=True)
    sink = grad_sink if grad_sink is not None else (lambda group, grads: None)
    tok = sink("ffn", dict(ffn_in=gw_ffn_in, ffn_down=gw_ffn_down))
    dx1, dg_ffn = _mm(dgu, w["ffn_in"], name="d_h2", tb=True, tk=1408, b_blocks=True, tm=1024, a_halves=True,
                      rms_bwd=(x1, p["norm_ffn_g"], dx2), after=tok)

    gw_out = _mm(merged, dx1, name="gw_out", ta=True, out_dtype=_CD)
    dy_a, dy_b, dgl, db_gate = _gate_bwd(dx1, w["out"], gl, p["b_gate"], y_a, y_b, name="gate_bwd")
    do_a = _mm(dy_a, w["dil_out"], name="d_o_a", tb=True)
    gw_dil_out = _mm(o_a, dy_a, name="gw_dil_out", ta=True, out_dtype=_CD, tn=1024)
    do_b = _mm(dy_b, w["fox_out"], name="d_o_b", tb=True)
    gw_fox_out = _mm(o_b, dy_b, name="gw_fox_out", ta=True, out_dtype=_CD, tn=1024)
    tok = sink("mix", dict(dil_out=gw_dil_out, fox_out=gw_fox_out, out=gw_out))

    bqp, bdop = _fox_pack_bwd(qkv, c, o_b, flse, do_b, name="fox_pack_bwd", after=tok)
    dqp, dkp, dvp, dck, dcq = _fox_bwd(bqp, fkp, fvp, bdop, name="fox_bwd")
    dc = dcq[:, 0, :].T - dck.reshape(S, N_FOX_HEADS, HEAD_DIM)[:, :, 0]
    dc = jnp.pad(dc, ((0, 0), (0, F_PAD - N_FOX_HEADS)))
    dzf, db_fgt = _fox_cumsum_bwd(dc, zf, p["b_fgt"], name="fox_cumsum_bwd")

    douts = _dil_mix_bwd(do_a, dil_o, dil_l, name="dil_mix_bwd", after=tok)
    dqs, dks, dvs = [], [], []
    for g in range(N_DIL_GROUPS):
        dq, dk, dv = _dil_bwd(dil_qkv[g], douts[3 + g], douts[g], g, name=f"dil_bwd{g}")
        for parts, t in ((dqs, dq), (dks, dk), (dvs, dv)):
            parts.extend([t[0].astype(_CD), t[1].astype(_CD)])
    dqkv = jnp.concatenate(dqs + dks + dvs + [dqp, dkp, dvp], axis=1)

    gw_qkv = _mm(h, dqkv, name="gw_qkv", ta=True, out_dtype=_CD, tn=768)
    gw_g = _mm(h, dgl, name="gw_gate", ta=True, out_dtype=_CD)
    gw_f = _mm(h, dzf, name="gw_f", ta=True, out_dtype=_CD)
    tok = sink("in", dict(qkv=gw_qkv, f=gw_f, g=gw_g))
    dx, dg_mix = _mm(dqkv, w["qkv"], name="d_h", tb=True, tk=QKV_COLS, tm=256, more=((dgl, w["g"]), (dzf, w["f"])),
                     rms_bwd=(x, p["norm_mix_g"], dx1), after=tok)

    gw = dict(qkv=gw_qkv, f=gw_f, g=gw_g, dil_out=gw_dil_out, fox_out=gw_fox_out, out=gw_out, ffn_in=gw_ffn_in,
              ffn_down=gw_ffn_down)
    small = dict(norm_mix_g=dg_mix, b_fgt=db_fgt, b_gate=db_gate, norm_ffn_g=dg_ffn, norm_final_g=dg_final)
    return loss, dx, gw, small


def _position():
    return lax.axis_index("x"), lax.axis_index("y"), lax.axis_index("c")


def _other_chips(x, y):
    return [(1 - x, y), (x, 1 - y), (1 - x, 1 - y)]


ROW_TILE = 16


def _row_chunks(rows, want=4):
    n = want
    while n > 1 and rows % (n * ROW_TILE):
        n //= 2
    return n


SEM_SPEC = pl.BlockSpec(memory_space=pltpu.SEMAPHORE)
ANY_SPEC = pl.BlockSpec(memory_space=pl.ANY)
DATAFLOW = pltpu.SideEffectType.DATAFLOW_SIDE_EFFECTING


def _in_hbm(a):
    return pltpu.with_memory_space_constraint(a, pltpu.HBM)


def _split_copy_start(srcs, land_shapes, copies, after, *, name):
    n, m = len(srcs), len(land_shapes)

    def body(*refs):
        src_refs, land_refs = refs[:n], refs[n:n + m]
        send_sems, recv_sems = refs[n + m + 1], refs[n + m + 2]
        token = refs[-1]
        x, y, c = _position()
        for k, (src, dst, peer) in enumerate(copies(x, y, c, src_refs, land_refs)):
            pltpu.make_async_remote_copy(src_ref=src, dst_ref=dst, send_sem=send_sems.at[k], recv_sem=recv_sems.at[k],
                                         device_id=peer, device_id_type=MESH).start()
        token[...] = jnp.zeros_like(token)

    lands = [lax.empty(s.shape, s.dtype) for s in land_shapes]
    count = len(copies(0, 0, 0, srcs, lands))
    out = _pcall(
        body, name=name,
        out_shape=(pltpu.SemaphoreType.DMA((count,)), pltpu.SemaphoreType.DMA((count,)),
                   *[pltpu.HBM(s.shape, s.dtype) for s in srcs], *[pltpu.HBM(s.shape, s.dtype) for s in land_shapes],
                   jax.ShapeDtypeStruct((8, 128), F32)),
        in_specs=[HBM_SPEC] * (n + m) + [ANY_SPEC],
        out_specs=(SEM_SPEC, SEM_SPEC, *[HBM_SPEC] * (n + m), pl.BlockSpec(memory_space=pltpu.VMEM)),
        input_output_aliases={k: 2 + k for k in range(n + m)},
        compiler_params=pltpu.CompilerParams(has_side_effects=DATAFLOW),
    )(*[_in_hbm(s) for s in srcs], *[_in_hbm(l) for l in lands], after)
    return out[0], out[1], list(out[2:2 + n]), list(out[2 + n:2 + n + m]), out[-1]


def _split_copy_wait(send_sems, recv_sems, srcs, lands, copies, after, *, name):
    n, m = len(srcs), len(lands)

    def body(*refs):
        src_refs, land_refs = refs[:n], refs[n:n + m]
        send, recv = refs[n + m], refs[n + m + 1]
        x, y, c = _position()
        for k, (src, dst, peer) in enumerate(copies(x, y, c, src_refs, land_refs)):
            cp = pltpu.make_async_remote_copy(src_ref=src, dst_ref=dst, send_sem=send.at[k], recv_sem=recv.at[k],
                                              device_id=peer, device_id_type=MESH)
            cp.wait_send()
            cp.wait_recv()

    afters = list(after) if isinstance(after, (list, tuple)) else [after]
    out = _pcall(
        body, name=name,
        out_shape=tuple(pltpu.HBM(s.shape, s.dtype) for s in list(srcs) + list(lands)),
        in_specs=[HBM_SPEC] * (n + m) + [SEM_SPEC, SEM_SPEC] + [ANY_SPEC] * len(afters),
        out_specs=tuple([HBM_SPEC] * (n + m)),
        input_output_aliases={k: k for k in range(n + m)},
        compiler_params=pltpu.CompilerParams(has_side_effects=DATAFLOW),
    )(*srcs, *lands, send_sems, recv_sems, *afters)
    return list(out[:n]), list(out[n:])


def _gather_copies(x, y, c, shard_refs, land_refs):
    out = []
    for s, l in zip(shard_refs, land_refs):
        half = s.shape[0] // 2
        nq = _row_chunks(half)
        for cx, cy in _other_chips(x, y):
            for q in range(nq):
                rows = pl.ds(c * half + q * (half // nq), half // nq)
                out.append((s.at[rows, :], l.at[2 * x + y, rows, :], (cx, cy, c)))
    return out


def _gather_whole_copies(x, y, c, shard_refs, land_refs):
    out = []
    for s, l in zip(shard_refs, land_refs):
        nq = _row_chunks(s.shape[0])
        for cx, cy in _other_chips(x, y):
            for q in range(nq):
                rows = pl.ds(q * (s.shape[0] // nq), s.shape[0] // nq)
                out.append((s.at[rows, :], l.at[2 * x + y, rows, :], (cx, cy, c)))
    return out


def _scatter_all_copies(x, y, c, block_refs, land_refs):
    out = []
    for g, l in zip(block_refs, land_refs):
        half = g.shape[1] // 2
        nq = _row_chunks(half)
        size = half // nq
        for q in range(nq):
            rows = pl.ds((1 - c) * half + q * size, size)
            out.append((g.at[2 * x + y, rows, :], l.at[0, pl.ds(q * size, size), :], (x, y, 1 - c)))
        for r, (cx, cy) in enumerate(_other_chips(x, y)):
            for j in range(2):
                h = c if j == 0 else 1 - c
                for q in range(nq):
                    rows = pl.ds(h * half + q * size, size)
                    out.append((g.at[2 * cx + cy, rows, :], l.at[1 + 2 * r + j, pl.ds(q * size, size), :], (cx, cy, h)))
    return out


def _forward_halves(lands, *, name):
    n = len(lands)

    def body(*refs):
        ins = refs[:n]
        send_sems, recv_sems = refs[2 * n:]
        x, y, c = _position()
        copies = []
        for w in range(n):
            half = ins[w].shape[1] // 2
            for r, (cx, cy) in enumerate(_other_chips(x, y)):
                blk = ins[w].at[2 * cx + cy, pl.ds(c * half, half), :]
                cp = pltpu.make_async_remote_copy(src_ref=blk, dst_ref=blk, send_sem=send_sems.at[w, r],
                                                  recv_sem=recv_sems.at[w, r], device_id=(x, y, 1 - c),
                                                  device_id_type=MESH)
                cp.start()
                copies.append(cp)
        for w in range(n):
            half = ins[w].shape[1] // 2
            for r, (cx, cy) in enumerate(_other_chips(x, y)):
                blk = ins[w].at[2 * cx + cy, pl.ds((1 - c) * half, half), :]
                pltpu.make_async_remote_copy(src_ref=blk, dst_ref=blk, send_sem=send_sems.at[w, r],
                                             recv_sem=recv_sems.at[w, r], device_id=(x, y, 1 - c),
                                             device_id_type=MESH).wait_recv()
        for cp in copies:
            cp.wait_send()

    return _pcall(
        body, name=name, in_specs=[HBM_SPEC] * n, out_specs=[HBM_SPEC] * n,
        out_shape=[jax.ShapeDtypeStruct(l.shape, l.dtype) for l in lands],
        input_output_aliases={k: k for k in range(n)},
        scratch_shapes=[pltpu.SemaphoreType.DMA((n, 3)), pltpu.SemaphoreType.DMA((n, 3))],
    )(*lands)


def _share_halves(halves, *, name):
    n = len(halves)

    def body(*refs):
        ins, outs = refs[:n], refs[n:2 * n]
        send_sems, recv_sems = refs[2 * n:]
        x, y, c = _position()
        copies = []
        for w in range(n):
            cp = pltpu.make_async_remote_copy(src_ref=ins[w], dst_ref=outs[w], send_sem=send_sems.at[w],
                                              recv_sem=recv_sems.at[w], device_id=(x, y, 1 - c), device_id_type=MESH)
            cp.start()
            copies.append(cp)
        for cp in copies:
            cp.wait()

    return _pcall(
        body, name=name, in_specs=[HBM_SPEC] * n, out_specs=[HBM_SPEC] * n,
        out_shape=[jax.ShapeDtypeStruct(h.shape, h.dtype) for h in halves],
        scratch_shapes=[pltpu.SemaphoreType.DMA((n,)), pltpu.SemaphoreType.DMA((n,))],
    )(*halves)


def _sum_small(part, after=None):
    rows, width = part.shape

    def body(x_ref, out_ref, all_ref, send_sems, recv_sems):
        x, y, c = _position()
        me, sibling = (x, y, c), (x, y, 1 - c)
        chips = _other_chips(x, y)

        def block(px, py, pc):
            return all_ref.at[pl.ds((4 * px + 2 * py + pc) * rows, rows), :]

        def copy(k, blk, to, src=None):
            return pltpu.make_async_remote_copy(
                src_ref=block(*blk) if src is None else src, dst_ref=block(*blk), send_sem=send_sems.at[k],
                recv_sem=recv_sems.at[k], device_id=to, device_id_type=MESH)

        all_ref[pl.ds((4 * x + 2 * y + c) * rows, rows), :] = x_ref[...]
        first = [copy(0, me, sibling, src=x_ref)]
        first += [copy(1 + j, me, (*chip, c), src=x_ref) for j, chip in enumerate(chips)]
        for cp in first:
            cp.start()
        passed = [copy(4 + j, (*chip, c), sibling) for j, chip in enumerate(chips)]
        for j, chip in enumerate(chips):
            copy(1 + j, (*chip, c), me).wait_recv()
            passed[j].start()
        copy(0, sibling, me).wait_recv()
        for j, chip in enumerate(chips):
            copy(4 + j, (*chip, 1 - c), me).wait_recv()
        for cp in first + passed:
            cp.wait_send()
        total = all_ref[0:rows, :]
        for d in range(1, 8):
            total = total + all_ref[d * rows:(d + 1) * rows, :]
        out_ref[...] = total

    vm = pl.BlockSpec(memory_space=pltpu.VMEM)
    return _pcall(
        body, after, name="sum_small", in_specs=[vm], out_specs=vm, out_shape=jax.ShapeDtypeStruct((rows, width), F32),
        scratch_shapes=[pltpu.VMEM((8 * rows, width), F32), pltpu.SemaphoreType.DMA((7,)), pltpu.SemaphoreType.DMA((7,))],
    )(part)


def _row_tile(R, C, itemsize=4, budget=1 << 20):
    for t in (512, 256, 128, 64, 32, 16, 8):
        if R % t == 0 and t * C * itemsize <= budget:
            return t
    return R


def _add_all(g, recv, where, *, name):
    _, R, C = g.shape
    half = R // 2
    t = _row_tile(half, C)
    nb = half // t

    def body(w_ref, g_ref, r_ref, o_ref):
        total = g_ref[0].astype(F32)
        for k in range(7):
            total = total + r_ref[k].astype(F32)
        o_ref[...] = total

    grid_spec = pltpu.PrefetchScalarGridSpec(
        num_scalar_prefetch=1, grid=(nb,),
        in_specs=[pl.BlockSpec((1, t, C), lambda i, wr: (wr[0], wr[1] * nb + i, 0)),
                  pl.BlockSpec((7, t, C), lambda i, wr: (0, i, 0))],
        out_specs=pl.BlockSpec((t, C), lambda i, wr: (i, 0)))
    return _pcall(body, name=name, grid_spec=grid_spec, out_shape=jax.ShapeDtypeStruct((half, C), F32),
                  compiler_params=_params("parallel"))(where, g, recv)


def _adamw(w, g, m, v, *, name):
    R, C = w.shape
    t = _row_tile(R, C)
    c1 = 1.0 - ADAM_B1 ** ADAM_STEP
    c2 = 1.0 - ADAM_B2 ** ADAM_STEP

    def body(w_ref, g_ref, m_ref, v_ref, d_ref, nm_ref, nv_ref):
        gv = g_ref[...]
        mn = ADAM_B1 * m_ref[...] + (1.0 - ADAM_B1) * gv
        vn = ADAM_B2 * v_ref[...] + (1.0 - ADAM_B2) * (gv * gv)
        d_ref[...] = -ADAM_LR * ((mn / c1) / (jnp.sqrt(vn / c2) + ADAM_EPS) + ADAM_WD * w_ref[...])
        nm_ref[...] = mn
        nv_ref[...] = vn

    blk = pl.BlockSpec((t, C), lambda i: (i, 0))
    shp = jax.ShapeDtypeStruct((R, C), F32)
    return _pcall(body, name=name, grid=(R // t,), in_specs=[blk] * 4, out_specs=[blk] * 3, out_shape=[shp] * 3,
                  compiler_params=_params("parallel"))(w, g, m, v)


def _adamw_halves(w, mine, theirs, m, v, core, *, name, after=None):
    R, C = w.shape
    half = R // 2
    t = _row_tile(half, C)
    nbh = half // t
    c1 = 1.0 - ADAM_B1 ** ADAM_STEP
    c2 = 1.0 - ADAM_B2 ** ADAM_STEP

    def body(core_ref, w_ref, a_ref, b_ref, m_ref, v_ref, *rest):
        g_ref, d_ref, nm_ref, nv_ref = rest[-4:]
        gv = jnp.where(pl.program_id(0) // nbh == core_ref[0], a_ref[...], b_ref[...])
        mn = ADAM_B1 * m_ref[...] + (1.0 - ADAM_B1) * gv
        vn = ADAM_B2 * v_ref[...] + (1.0 - ADAM_B2) * (gv * gv)
        g_ref[...] = gv
        d_ref[...] = -ADAM_LR * ((mn / c1) / (jnp.sqrt(vn / c2) + ADAM_EPS) + ADAM_WD * w_ref[...])
        nm_ref[...] = mn
        nv_ref[...] = vn

    blk = pl.BlockSpec((t, C), lambda i, cr: (i, 0))
    hblk = pl.BlockSpec((t, C), lambda i, cr: (i % nbh, 0))
    shp = jax.ShapeDtypeStruct((R, C), F32)
    tied = [] if after is None else [after]
    grid_spec = pltpu.PrefetchScalarGridSpec(num_scalar_prefetch=1, grid=(2 * nbh,),
                                             in_specs=[blk, hblk, hblk, blk, blk] + [ANY_SPEC] * len(tied),
                                             out_specs=[blk] * 4)
    return _pcall(body, name=name, grid_spec=grid_spec, out_shape=[shp] * 4,
                  compiler_params=_params("parallel"))(core, w, mine, theirs, m, v, *tied)


BIG = ("w_in", "w_dil_out", "w_fox_out", "w_out", "w_ffn_in", "w_ffn_down")
SMALL = ("norm_mix_g", "b_fgt", "b_gate", "norm_ffn_g", "norm_final_g")
ORDER = ("norm_mix_g", "w_in", "b_fgt", "b_gate", "w_dil_out", "w_fox_out", "w_out", "norm_ffn_g", "w_ffn_in",
         "w_ffn_down", "norm_final_g")
SMALL_ROWS = {"norm_mix_g": (0, 1), "b_gate": (1, 3), "norm_ffn_g": (3, 4), "norm_final_g": (4, 5), "b_fgt": (5, 6)}


def _columns_to_blocks(full, ncol):
    K = full.shape[0]
    return full.reshape(K, 4, ncol).transpose(1, 0, 2)


def _pieces_to_blocks(pieces, ncol):
    spans, start = [], 0
    for piece in pieces:
        spans.append((piece, start, start + piece.shape[1]))
        start += piece.shape[1]
    assert start == 4 * ncol
    blocks = []
    for k in range(4):
        lo, hi = k * ncol, (k + 1) * ncol
        parts = [p[:, max(lo, a) - a:min(hi, b) - a] for p, a, b in spans if a < hi and b > lo]
        blocks.append(parts[0] if len(parts) == 1 else jnp.concatenate(parts, axis=1))
    return jnp.stack(blocks)


def _blocks_to_columns(blocks):
    n, K, ncol = blocks.shape
    return blocks.transpose(1, 0, 2).reshape(K, n * ncol)


def kernel(x, norm_mix_g, w_in, b_fgt, b_gate, w_dil_out, w_fox_out, w_out, norm_ffn_g, w_ffn_in, w_ffn_down, norm_final_g, loss_target, m_norm_mix_g, m_w_in, m_b_fgt, m_b_gate, m_w_dil_out, m_w_fox_out, m_w_out, m_norm_ffn_g, m_w_ffn_in, m_w_ffn_down, m_norm_final_g, v_norm_mix_g, v_w_in, v_b_fgt, v_b_gate, v_w_dil_out, v_w_fox_out, v_w_out, v_norm_ffn_g, v_w_ffn_in, v_w_ffn_down, v_norm_final_g):
    weights = dict(norm_mix_g=norm_mix_g, w_in=w_in, b_fgt=b_fgt, b_gate=b_gate, w_dil_out=w_dil_out,
                   w_fox_out=w_fox_out, w_out=w_out, norm_ffn_g=norm_ffn_g, w_ffn_in=w_ffn_in, w_ffn_down=w_ffn_down,
                   norm_final_g=norm_final_g)
    m_in = dict(norm_mix_g=m_norm_mix_g, w_in=m_w_in, b_fgt=m_b_fgt, b_gate=m_b_gate, w_dil_out=m_w_dil_out,
                w_fox_out=m_w_fox_out, w_out=m_w_out, norm_ffn_g=m_norm_ffn_g, w_ffn_in=m_w_ffn_in,
                w_ffn_down=m_w_ffn_down, norm_final_g=m_norm_final_g)
    v_in = dict(norm_mix_g=v_norm_mix_g, w_in=v_w_in, b_fgt=v_b_fgt, b_gate=v_b_gate, w_dil_out=v_w_dil_out,
                w_fox_out=v_w_fox_out, w_out=v_w_out, norm_ffn_g=v_norm_ffn_g, w_ffn_in=v_w_ffn_in,
                w_ffn_down=v_w_ffn_down, norm_final_g=v_norm_final_g)
    c = lax.axis_index("c")
    chip = 2 * lax.axis_index("x") + lax.axis_index("y")

    shards = {n: weights[n][0].astype(_CD) for n in BIG}
    in_shape = jax.ShapeDtypeStruct((4,) + shards["w_in"].shape, _CD)
    send_i, recv_i, in_src, in_land, token_in = _split_copy_start(
        [shards["w_in"]], [in_shape], _gather_copies, norm_mix_g, name="gather_in_start")
    late = BIG[1:]
    send_g, recv_g, late_src, late_land, token = _split_copy_start(
        [shards[n] for n in late], [jax.ShapeDtypeStruct((4,) + shards[n].shape, _CD) for n in late],
        _gather_whole_copies, token_in, name="gather_late_start")
    adam_in = [t[0] + token_in[0, 0] for t in (w_in, m_w_in, v_w_in)]
    p = dict(norm_mix_g=norm_mix_g, b_fgt=jnp.pad(b_fgt, ((0, 0), (0, F_PAD - N_FOX_HEADS))), b_gate=b_gate,
             norm_ffn_g=norm_ffn_g, norm_final_g=norm_final_g.reshape(1, D_MODEL))

    def first_weights(after):
        own, lands = _split_copy_wait(send_i, recv_i, in_src, in_land, _gather_copies, [after] + adam_in,
                                      name="gather_in_wait")
        (g_in,) = _forward_halves(lands, name="gather_in_forward")
        full_in = _blocks_to_columns(lax.dynamic_update_index_in_dim(g_in, own[0], chip, 0))
        o3 = QKV_COLS
        o4 = o3 + N_FOX_HEADS
        return dict(qkv=full_in[:, :o3], f=jnp.pad(full_in[:, o3:o4], ((0, 0), (0, F_PAD - N_FOX_HEADS))),
                    g=full_in[:, o4:])

    def late_weights(after):
        own, lands = _split_copy_wait(send_g, recv_g, late_src, late_land, _gather_whole_copies, after,
                                      name="gather_late_wait")
        g_dil, g_fox, g_out, g_ffn_in, g_ffn_down = [
            lax.dynamic_update_index_in_dim(l, s, chip, 0) for l, s in zip(lands, own)]
        return dict(dil_out=_blocks_to_columns(g_dil), fox_out=_blocks_to_columns(g_fox),
                    out=g_out.reshape(D_MODEL, D_MODEL), ffn_in=g_ffn_in,
                    ffn_down=g_ffn_down.reshape(D_FF, D_MODEL))

    def to_blocks(n, full):
        shape = weights[n].shape
        if full.ndim == 3:
            return full
        if n in ("w_out", "w_ffn_down"):
            return full.reshape(4, shape[1], shape[2])
        return _columns_to_blocks(full, shape[2])

    in_flight = {}

    def grad_sink(group, gw):
        if group == "in":
            named = {"w_in": _pieces_to_blocks([gw["qkv"], gw["f"][:, :N_FOX_HEADS], gw["g"]], weights["w_in"].shape[2])}
        else:
            named = {"w_" + k: v for k, v in gw.items()}
        srcs = [to_blocks(n, named[n]) for n in named]
        lands = [jax.ShapeDtypeStruct((7, s.shape[1] // 2, s.shape[2]), s.dtype) for s in srcs]
        started = _split_copy_start(srcs, lands, _scatter_all_copies, next(iter(gw.values())),
                                    name=f"scatter_{group}_start")
        in_flight[group] = (list(named), started)
        return started[-1]

    loss_part, grad_x, gw, small = _layer_step(x[0], loss_target[0], {}, p, late_weights, grad_sink,
                                               (token_in, token), first_weights)

    where = jnp.stack([chip, c]).astype(jnp.int32)

    def summed_halves(groups, after, name):
        halves = {}
        for group in groups:
            names, (send_s, recv_s, srcs, lands, _) = in_flight[group]
            srcs, recv = _split_copy_wait(send_s, recv_s, srcs, lands, _scatter_all_copies, after,
                                          name=f"scatter_{group}_wait")
            halves.update({n: _add_all(s, r, where, name=f"add_all_{n}") for n, s, r in zip(names, srcs, recv)})
        return {n: (h, o) for (n, h), o in zip(halves.items(), _share_halves(list(halves.values()), name=name))}

    grad_halves = summed_halves(("ffn", "mix"), grad_x, "share_halves")

    out_g, out_d, out_m, out_v = {}, {}, {}, {}
    core = jnp.reshape(c, (1,)).astype(jnp.int32)

    def adamw_big(n, after=None):
        shape = weights[n].shape
        wmv = adam_in if n == "w_in" else [t[0] for t in (weights[n], m_in[n], v_in[n])]
        mine, theirs = grad_halves[n]
        outs = _adamw_halves(wmv[0], mine, theirs, wmv[1], wmv[2], core, name=f"adamw_{n}", after=after)
        out_g[n], out_d[n], out_m[n], out_v[n] = [t.reshape(shape) for t in outs]

    early = ("w_dil_out", "w_fox_out", "w_out", "w_ffn_down")
    for n in early:
        adamw_big(n)
    grad_halves.update(summed_halves(("in",), [grad_x] + [out_d[n] for n in early], "share_halves_in"))
    adamw_big("w_in")

    packed = jnp.concatenate([
        small["norm_mix_g"], small["b_gate"].reshape(2, D_MODEL), small["norm_ffn_g"], small["norm_final_g"],
        jnp.pad(small["b_fgt"], ((0, 0), (0, D_MODEL - F_PAD))), jnp.pad(loss_part, ((0, 0), (0, D_MODEL - 1))),
        jnp.zeros((1, D_MODEL), F32)], axis=0)
    summed = _sum_small(packed, after=out_d["w_in"])
    loss = summed[6, 0]
    adamw_big("w_ffn_in", after=summed)

    for n in SMALL:
        lo, hi = SMALL_ROWS[n]
        shape = weights[n].shape
        g2 = summed[lo:hi].reshape(1, -1)[:, :weights[n].size]
        d2, m2, v2 = _adamw(weights[n].reshape(g2.shape), g2, m_in[n].reshape(g2.shape), v_in[n].reshape(g2.shape),
                            name=f"adamw_{n}")
        out_g[n], out_d[n], out_m[n], out_v[n] = [t.reshape(shape) for t in (g2, d2, m2, v2)]
    return (loss, grad_x[None], *[out_g[n] for n in ORDER], *[out_d[n] for n in ORDER],
            *[out_m[n] for n in ORDER], *[out_v[n] for n in ORDER])
```

```python
import numpy as np
import jax
import jax.numpy as jnp
from jax import lax
from jax.experimental import pallas as pl
from jax.experimental.pallas import tpu as pltpu

F32 = jnp.float32
_CD = jnp.bfloat16

D_MODEL = 1024
HEAD_DIM = 64
DIL_PAIRS = ((128, 1), (512, 4), (2048, 16))
N_DIL_GROUPS = 3
DIL_HEADS = 4
DIL_W = 128
DIL_OUT = DIL_HEADS * HEAD_DIM
DIL_WIDTH = N_DIL_GROUPS * DIL_OUT
N_FOX_HEADS = 8
FOX_WIDTH = N_FOX_HEADS * HEAD_DIM
D_FF = 2816
QKV_COLS = 3 * DIL_WIDTH + 3 * FOX_WIDTH
F_PAD = 128
RMS_EPS = 1e-6
NEG_INF = -1e30
ATTN_SCALE = HEAD_DIM ** -0.5
ADAM_LR, ADAM_B1, ADAM_B2, ADAM_EPS, ADAM_WD, ADAM_STEP = 0.001, 0.9, 0.999, 1e-08, 0.01, 10

VMEM_LIMIT = 48 * 1024 * 1024
VMEM_LIMIT_RESIDENT = 56 * 1024 * 1024
LANES = 128
MESH = pl.DeviceIdType.MESH
HBM_SPEC = pl.BlockSpec(memory_space=pltpu.HBM)


def _pcall(body, after=None, **kw):
    if after is None:
        return pl.pallas_call(body, **kw)
    n_in = len(kw["in_specs"])
    kw["in_specs"] = list(kw["in_specs"]) + [pl.BlockSpec(memory_space=pl.ANY)]

    def tied(*refs):
        return body(*refs[:n_in], *refs[n_in + 1:])

    call = pl.pallas_call(tied, **kw)
    return lambda *args: call(*args, after)


def _params(*sem):
    return pltpu.CompilerParams(dimension_semantics=sem, vmem_limit_bytes=VMEM_LIMIT)


def _pick(dim, pref):
    t = (min(pref, dim) // 128) * 128
    while t >= 128:
        if dim % t == 0:
            return t
        t -= 128
    return dim


def _mm(a, b, *, name, ta=False, tb=False, out_dtype=F32, add=None, tm=1024, tn=512, tk=2048, after=None,
        b_blocks=False, out_blocks=None, a_halves=False, b_halves=False, rms_bwd=None, more=()):
    if a_halves:
        M, K = a.shape[1], 2 * a.shape[2]
    elif ta:
        K, M = a.shape
    else:
        M, K = a.shape
    if b_halves:
        b_rows, b_cols = b.shape[1], 2 * b.shape[2]
    else:
        b_rows, b_cols = (b.shape[1], b.shape[0] * b.shape[2]) if b_blocks else b.shape
    if tb:
        N, K2 = b_rows, b_cols
    else:
        K2, N = b_rows, b_cols
    assert K == K2, (a.shape, b.shape)
    shard = b.shape[2] if b_blocks else None
    tm = _pick(M, tm)
    tn = _pick(shard if (b_blocks and not tb) else (out_blocks or N), tn)
    tk = _pick(shard if (b_blocks and tb) else K, tk)
    nk = K // tk
    dn = (((0 if ta else 1,), (1 if tb else 0,)), ((), ()))
    has_add = add is not None
    assert not (has_add and out_blocks)
    has_norm = rms_bwd is not None
    if has_norm:
        tn = N
        assert not out_blocks and out_dtype == F32
    assert not more or (nk == 1 and tb and not ta)

    def body(*refs):
        a_ref, b_ref = refs[0], refs[1]
        rest = list(refs[2:])
        more_refs = [(rest.pop(0), rest.pop(0)) for _ in more]
        add_ref = rest.pop(0) if has_add else None
        x_ref, g_ref, dres_ref = (rest.pop(0), rest.pop(0), rest.pop(0)) if has_norm else (None, None, None)
        o_ref = rest.pop(0)
        dg_ref = rest.pop(0) if has_norm else None
        bv = b_ref[0] if b_blocks else b_ref[...]
        p = lax.dot_general(a_ref[...].astype(_CD), bv.astype(_CD), dn, preferred_element_type=F32)
        for a2_ref, b2_ref in more_refs:
            p += lax.dot_general(a2_ref[...].astype(_CD), b2_ref[...].astype(_CD), dn, preferred_element_type=F32)

        def finish(r):
            if has_add:
                r = r + add_ref[...]
            if has_norm:
                xv = x_ref[...]
                rs = lax.rsqrt(jnp.mean(xv * xv, axis=-1, keepdims=True) + RMS_EPS)
                xh = xv * rs
                dxh = r * g_ref[...]
                o_ref[...] = dres_ref[...] + rs * (dxh - xh * jnp.mean(dxh * xh, axis=-1, keepdims=True))
                part = jnp.sum(r * xh, axis=0, keepdims=True)
                first = pl.program_id(0) == 0

                @pl.when(first)
                def _():
                    dg_ref[...] = part

                @pl.when(jnp.logical_not(first))
                def _():
                    dg_ref[...] += part
            elif out_blocks:
                o_ref[0] = r.astype(out_dtype)
            else:
                o_ref[...] = r.astype(out_dtype)

        if nk == 1:
            finish(p)
        else:
            acc_ref = rest.pop(0)
            k = pl.program_id(2)

            @pl.when(k == 0)
            def _():
                acc_ref[...] = p

            @pl.when(k > 0)
            def _():
                acc_ref[...] += p

            @pl.when(k == nk - 1)
            def _():
                finish(acc_ref[...])

    if a_halves:
        ka = (K // 2) // tk
        a_spec = pl.BlockSpec((None, tm, tk), lambda i, j, k: (k // ka, i, k % ka))
    else:
        a_spec = pl.BlockSpec((tk, tm), lambda i, j, k: (k, i)) if ta else pl.BlockSpec((tm, tk), lambda i, j, k: (i, k))
    if b_halves:
        nb_ = (N // 2) // tn
        b_spec = pl.BlockSpec((None, tk, tn), lambda i, j, k: (j // nb_, k, j % nb_))
    elif b_blocks and tb:
        per = shard // tk
        b_spec = pl.BlockSpec((1, tn, tk), lambda i, j, k: (k // per, j, k % per))
    elif b_blocks:
        per = shard // tn
        b_spec = pl.BlockSpec((1, tk, tn), lambda i, j, k: (j // per, k, j % per))
    else:
        b_spec = pl.BlockSpec((tn, tk), lambda i, j, k: (j, k)) if tb else pl.BlockSpec((tk, tn), lambda i, j, k: (k, j))
    if out_blocks:
        oper = out_blocks // tn
        o_spec = pl.BlockSpec((1, tm, tn), lambda i, j, k: (j // oper, i, j % oper))
        out_shape = jax.ShapeDtypeStruct((N // out_blocks, M, out_blocks), out_dtype)
    else:
        o_spec = pl.BlockSpec((tm, tn), lambda i, j, k: (i, j))
        out_shape = jax.ShapeDtypeStruct((M, N), out_dtype)
    in_specs, args = [a_spec, b_spec], (a, b)
    for a2, b2 in more:
        assert a2.shape[0] == M and b2.shape == (N, a2.shape[1]), (a2.shape, b2.shape)
        in_specs += [pl.BlockSpec((tm, a2.shape[1]), lambda i, j, k: (i, 0)),
                     pl.BlockSpec((tn, a2.shape[1]), lambda i, j, k: (j, 0))]
        args += (a2, b2)
    if has_add:
        in_specs, args = in_specs + [o_spec], args + (add,)
    out_specs, semantics = o_spec, ("parallel", "parallel", "arbitrary")
    if has_norm:
        vec = pl.BlockSpec((1, N), lambda i, j, k: (0, 0))
        in_specs += [o_spec, vec, o_spec]
        args += tuple(rms_bwd)
        out_specs, out_shape = [o_spec, vec], [out_shape, jax.ShapeDtypeStruct((1, N), F32)]
        semantics = ("arbitrary", "arbitrary", "arbitrary")
    return _pcall(
        body, after, name=name, grid=(M // tm, N // tn, nk), in_specs=in_specs, out_specs=out_specs,
        out_shape=out_shape,
        scratch_shapes=[pltpu.VMEM((tm, tn), F32)] if nk > 1 else [],
        compiler_params=_params(*semantics),
    )(*args)


def _rms_fwd(x, g, *, name, tm=512, after=None):
    S, D = x.shape

    def body(x_ref, g_ref, h_ref):
        xv = x_ref[...]
        r = lax.rsqrt(jnp.mean(xv * xv, axis=-1, keepdims=True) + RMS_EPS)
        h_ref[...] = ((xv * r) * g_ref[...]).astype(h_ref.dtype)

    row = pl.BlockSpec((tm, D), lambda i: (i, 0))
    return _pcall(body, after, name=name, grid=(S // tm,), in_specs=[row, pl.BlockSpec((1, D), lambda i: (0, 0))],
                  out_specs=row, out_shape=jax.ShapeDtypeStruct((S, D), _CD), compiler_params=_params("parallel"))(x, g)


def _ffn_down_loss(act, w_down, x1, g, tgt, *, name, tm=512):
    S, D = x1.shape
    F = act.shape[1]

    def body(a_ref, b_ref, x_ref, g_ref, t_ref, loss_ref, dx_ref, dg_ref):
        xv = x_ref[...] + jnp.dot(a_ref[...].astype(_CD), b_ref[...].astype(_CD), preferred_element_type=F32)
        gv = g_ref[...]
        r = lax.rsqrt(jnp.mean(xv * xv, axis=-1, keepdims=True) + RMS_EPS)
        xh = xv * r
        err = xh * gv - t_ref[...]
        lpart = 0.5 * jnp.sum(jnp.mean(err * err, axis=-1, keepdims=True), axis=0, keepdims=True)
        dy = err * (1.0 / D)
        dxh = dy * gv
        dx_ref[...] = r * (dxh - xh * jnp.mean(dxh * xh, axis=-1, keepdims=True))
        gpart = jnp.sum(dy * xh, axis=0, keepdims=True)

        @pl.when(pl.program_id(0) == 0)
        def _():
            loss_ref[...] = lpart
            dg_ref[...] = gpart

        @pl.when(pl.program_id(0) > 0)
        def _():
            loss_ref[...] += lpart
            dg_ref[...] += gpart

    row = pl.BlockSpec((tm, D), lambda i: (i, 0))
    vec = pl.BlockSpec((1, D), lambda i: (0, 0))
    one = pl.BlockSpec((1, 1), lambda i: (0, 0))
    return _pcall(body, name=name, grid=(S // tm,),
                  in_specs=[pl.BlockSpec((tm, F), lambda i: (i, 0)), pl.BlockSpec((F, D), lambda i: (0, 0)), row, vec, row],
                  out_specs=[one, row, vec],
                  out_shape=[jax.ShapeDtypeStruct((1, 1), F32), jax.ShapeDtypeStruct((S, D), F32),
                             jax.ShapeDtypeStruct((1, D), F32)],
                  compiler_params=_params("arbitrary"))(act, w_down, x1, g, tgt)


def _sigmoid(z):
    return 1.0 / (1.0 + jnp.exp(-z))


def _gated_mix_out(gl, bg, ya, yb, w_out, x, g, *, name, tm=512):
    S, D = ya.shape

    def body(za_ref, zb_ref, ba_ref, bb_ref, ya_ref, yb_ref, w_ref, x_ref, g_ref, m_ref, x1_ref, h_ref):
        ga = _sigmoid(za_ref[...].astype(F32) + ba_ref[...])
        gb = _sigmoid(zb_ref[...].astype(F32) + bb_ref[...])
        merged = (ga * ya_ref[...].astype(F32) + gb * yb_ref[...].astype(F32)).astype(m_ref.dtype)
        m_ref[...] = merged
        x1 = x_ref[...] + jnp.dot(merged, w_ref[...].astype(_CD), preferred_element_type=F32)
        x1_ref[...] = x1
        rs = lax.rsqrt(jnp.mean(x1 * x1, axis=-1, keepdims=True) + RMS_EPS)
        h_ref[...] = ((x1 * rs) * g_ref[...]).astype(h_ref.dtype)

    lo = pl.BlockSpec((tm, D), lambda i: (i, 0))
    hi = pl.BlockSpec((tm, D), lambda i: (i, 1))
    vlo = pl.BlockSpec((1, D), lambda i: (0, 0))
    vhi = pl.BlockSpec((1, D), lambda i: (0, 1))
    whole = pl.BlockSpec((D, D), lambda i: (0, 0))
    return _pcall(body, name=name, grid=(S // tm,), in_specs=[lo, hi, vlo, vhi, lo, lo, whole, lo, vlo],
                  out_specs=[lo, lo, lo],
                  out_shape=[jax.ShapeDtypeStruct((S, D), _CD), jax.ShapeDtypeStruct((S, D), F32),
                             jax.ShapeDtypeStruct((S, D), _CD)],
                  compiler_params=_params("parallel"))(gl, gl, bg, bg, ya, yb, w_out, x, g)


def _gate_bwd(dx1, w_out, gl, bg, ya, yb, *, name, tm=512):
    S, D = ya.shape
    nt = (((1,), (1,)), ((), ()))

    def body(dx_ref, w_ref, za_ref, zb_ref, ba_ref, bb_ref, ya_ref, yb_ref, dya_ref, dyb_ref, dgl_ref, dbg_ref):
        dmv = lax.dot_general(dx_ref[...].astype(_CD), w_ref[...].astype(_CD), nt, preferred_element_type=F32)
        ga = _sigmoid(za_ref[...].astype(F32) + ba_ref[...])
        gb = _sigmoid(zb_ref[...].astype(F32) + bb_ref[...])
        dya_ref[...] = (dmv * ga).astype(dya_ref.dtype)
        dyb_ref[...] = (dmv * gb).astype(dyb_ref.dtype)
        dza = dmv * ya_ref[...].astype(F32) * ga * (1.0 - ga)
        dzb = dmv * yb_ref[...].astype(F32) * gb * (1.0 - gb)
        dgl_ref[:, :D] = dza.astype(dgl_ref.dtype)
        dgl_ref[:, D:] = dzb.astype(dgl_ref.dtype)
        pa = jnp.sum(dza, axis=0, keepdims=True)
        pb = jnp.sum(dzb, axis=0, keepdims=True)

        @pl.when(pl.program_id(0) == 0)
        def _():
            dbg_ref[:, :D] = pa
            dbg_ref[:, D:] = pb

        @pl.when(pl.program_id(0) > 0)
        def _():
            dbg_ref[:, :D] += pa
            dbg_ref[:, D:] += pb

    lo = pl.BlockSpec((tm, D), lambda i: (i, 0))
    hi = pl.BlockSpec((tm, D), lambda i: (i, 1))
    vlo = pl.BlockSpec((1, D), lambda i: (0, 0))
    vhi = pl.BlockSpec((1, D), lambda i: (0, 1))
    wide = pl.BlockSpec((tm, 2 * D), lambda i: (i, 0))
    vwide = pl.BlockSpec((1, 2 * D), lambda i: (0, 0))
    whole = pl.BlockSpec((D, D), lambda i: (0, 0))
    return _pcall(body, name=name, grid=(S // tm,), in_specs=[lo, whole, lo, hi, vlo, vhi, lo, lo],
                  out_specs=[lo, lo, wide, vwide],
                  out_shape=[jax.ShapeDtypeStruct((S, D), _CD), jax.ShapeDtypeStruct((S, D), _CD),
                             jax.ShapeDtypeStruct((S, 2 * D), _CD), jax.ShapeDtypeStruct((1, 2 * D), F32)],
                  compiler_params=_params("arbitrary"))(dx1, w_out, gl, gl, bg, bg, ya, yb)


def _ffn_in_act(h2, w_blocks, *, name, tm=512):
    S, D = h2.shape
    _, _, C = w_blocks.shape

    def body(a_ref, bg_ref, bu_ref, g_ref, u_ref, o_ref):
        av = a_ref[...].astype(_CD)
        gv = jnp.dot(av, bg_ref[0].astype(_CD), preferred_element_type=F32)
        uv = jnp.dot(av, bu_ref[0].astype(_CD), preferred_element_type=F32)
        g_ref[...] = gv.astype(g_ref.dtype)
        u_ref[...] = uv.astype(u_ref.dtype)
        o_ref[...] = (gv * _sigmoid(gv) * uv).astype(o_ref.dtype)

    out = pl.BlockSpec((tm, C), lambda i, j: (i, j))
    shp = jax.ShapeDtypeStruct((S, 2 * C), _CD)
    return _pcall(body, name=name, grid=(S // tm, 2),
                  in_specs=[pl.BlockSpec((tm, D), lambda i, j: (i, 0)), pl.BlockSpec((1, D, C), lambda i, j: (j, 0, 0)),
                            pl.BlockSpec((1, D, C), lambda i, j: (2 + j, 0, 0))],
                  out_specs=[out, out, out], out_shape=[shp, shp, shp],
                  compiler_params=_params("parallel", "arbitrary"))(h2, w_blocks, w_blocks)


def _d_swiglu(dx, w_down, gate, up, *, name, tm=512, tn=1408):
    S, D = dx.shape
    F = w_down.shape[0]
    nt = (((1,), (1,)), ((), ()))

    def body(a_ref, b_ref, g_ref, u_ref, o_ref):
        dv = lax.dot_general(a_ref[...].astype(_CD), b_ref[...].astype(_CD), nt, preferred_element_type=F32)
        gv = g_ref[...].astype(F32)
        sg = _sigmoid(gv)
        o_ref[0] = (dv * u_ref[...].astype(F32) * (sg * (1.0 + gv * (1.0 - sg)))).astype(o_ref.dtype)
        o_ref[1] = (dv * (gv * sg)).astype(o_ref.dtype)

    tile = pl.BlockSpec((tm, tn), lambda i, j: (i, j))
    return _pcall(body, name=name, grid=(S // tm, F // tn),
                  in_specs=[pl.BlockSpec((tm, D), lambda i, j: (i, 0)), pl.BlockSpec((tn, D), lambda i, j: (j, 0)),
                            tile, tile],
                  out_specs=pl.BlockSpec((2, tm, tn), lambda i, j: (0, i, j)),
                  out_shape=jax.ShapeDtypeStruct((2, S, F), _CD),
                  compiler_params=_params("parallel", "arbitrary"))(dx, w_down, gate, up)


def _split3(x):
    hi = x.astype(jnp.bfloat16)
    r1 = x - hi.astype(F32)
    mid = r1.astype(jnp.bfloat16)
    lo = (r1 - mid.astype(F32)).astype(jnp.bfloat16)
    return hi, mid, lo


def _ones_dot_left(ones, x):
    return sum(jnp.dot(ones, p, preferred_element_type=F32) for p in _split3(x))


def _ones_dot_right(x, ones):
    return sum(jnp.dot(p, ones, preferred_element_type=F32) for p in _split3(x))


def _head_sum(x):
    n = x.shape[1]
    r = lax.broadcasted_iota(jnp.int32, (n, n), 0) // HEAD_DIM
    c = lax.broadcasted_iota(jnp.int32, (n, n), 1) // HEAD_DIM
    return _ones_dot_right(x, (r == c).astype(jnp.bfloat16))


def _log_sigmoid(z):
    e = jnp.exp(-jnp.abs(z))
    t = 1.0 + e
    log1p_e = jnp.where(t == 1.0, e, jnp.log(t) * (e / jnp.where(t == 1.0, 1.0, t - 1.0)))
    return jnp.minimum(z, 0.0) - log1p_e


def _fox_cumsum(zf, bf, *, name):
    S, W = zf.shape
    nb = S // 128

    def body(z_ref, b_ref, c_ref):
        tri = (lax.broadcasted_iota(jnp.int32, (128, 128), 0) >= lax.broadcasted_iota(jnp.int32, (128, 128), 1))
        tri = tri.astype(jnp.bfloat16)

        def step(i, carry):
            rows = pl.ds(pl.multiple_of(i * 128, 128), 128)
            lf = _log_sigmoid(z_ref[rows, :] + b_ref[...])
            cb = _ones_dot_left(tri, lf) + carry
            c_ref[rows, :] = cb
            return cb[127:128, :]

        lax.fori_loop(0, nb, step, jnp.zeros((1, W), F32))

    return _pcall(body, name=name, out_shape=jax.ShapeDtypeStruct((S, W), F32),
                  compiler_params=pltpu.CompilerParams(vmem_limit_bytes=VMEM_LIMIT))(zf, bf)


def _fox_cumsum_bwd(dc, zf, bf, *, name):
    S, W = zf.shape
    nb = S // 128

    def body(dc_ref, z_ref, b_ref, dz_ref, db_ref):
        tri = (lax.broadcasted_iota(jnp.int32, (128, 128), 0) <= lax.broadcasted_iota(jnp.int32, (128, 128), 1))
        tri = tri.astype(jnp.bfloat16)

        def step(k, carry):
            tail, acc = carry
            i = nb - 1 - k
            rows = pl.ds(pl.multiple_of(i * 128, 128), 128)
            dlf = _ones_dot_left(tri, dc_ref[rows, :]) + tail
            dz = dlf * _sigmoid(-(z_ref[rows, :] + b_ref[...]))
            dz_ref[rows, :] = dz
            return dlf[0:1, :], acc + jnp.sum(dz, axis=0, keepdims=True)

        _, acc = lax.fori_loop(0, nb, step, (jnp.zeros((1, W), F32), jnp.zeros((1, W), F32)))
        db_ref[...] = acc

    return _pcall(body, name=name,
                  out_shape=[jax.ShapeDtypeStruct((S, W), F32), jax.ShapeDtypeStruct((1, W), F32)],
                  compiler_params=pltpu.CompilerParams(vmem_limit_bytes=VMEM_LIMIT))(dc, zf, bf)


def _proj_dil(h, w_qkv, *, name, tm=1024):
    S, D = h.shape
    tn = DIL_WIDTH

    def body(a_ref, b_ref, *rest):
        outs, acc = rest[:N_DIL_GROUPS], rest[N_DIL_GROUPS]
        prod = jnp.dot(a_ref[...].astype(_CD), b_ref[...].astype(_CD), preferred_element_type=F32)
        for k in range(tn // LANES):
            acc[k] = prod[:, k * LANES:(k + 1) * LANES]
        for g, (_, d) in enumerate(DIL_PAIRS):
            for half in range(DIL_OUT // LANES):
                k = g * (DIL_OUT // LANES) + half
                cols = slice(half * LANES, (half + 1) * LANES)
                for r in range(d):
                    rows = pl.ds(r, tm // d, stride=d) if d > 1 else slice(None)
                    outs[g][0, r, :, cols] = acc[k, rows, :].astype(outs[g].dtype)

    out_specs = [pl.BlockSpec((1, d, tm // d, DIL_OUT), lambda i, j: (j, 0, i, 0)) for _, d in DIL_PAIRS]
    out_shape = [jax.ShapeDtypeStruct((3, d, S // d, DIL_OUT), _CD) for _, d in DIL_PAIRS]
    outs = _pcall(body, name=name, grid=(S // tm, 3),
                  in_specs=[pl.BlockSpec((tm, D), lambda i, j: (i, 0)), pl.BlockSpec((D, tn), lambda i, j: (0, j))],
                  out_specs=out_specs, out_shape=out_shape, scratch_shapes=[pltpu.VMEM((tn // LANES, tm, LANES), F32)],
                  compiler_params=_params("parallel", "arbitrary"))(h, w_qkv)
    return [o.reshape(3, S, DIL_OUT) for o in outs]


def _dil_start(block, S, dilation):
    sub = S // dilation
    u0 = block * DIL_W
    return (u0 % sub) * dilation + u0 // sub


def _dil_slopes(group):
    h = np.arange(1, N_DIL_GROUPS * DIL_HEADS + 1, dtype=np.float32)
    s = (np.float32(2.0) ** (np.float32(-8.0) * h / np.float32(N_DIL_GROUPS * DIL_HEADS))).astype(np.float32)
    return [float(v) for v in s.reshape(N_DIL_GROUPS, DIL_HEADS)[group]]


def _dil_tiles(i, n, blocks_per_seq):
    qi = lax.broadcasted_iota(jnp.int32, (DIL_W, 2 * DIL_W), 0)
    kj = lax.broadcasted_iota(jnp.int32, (DIL_W, 2 * DIL_W), 1)
    rel = qi + DIL_W - kj
    first = ((4 * n + i) % blocks_per_seq) == 0
    valid = jnp.logical_and(jnp.logical_and(rel >= 0, rel <= DIL_W), jnp.logical_or(kj >= DIL_W, jnp.logical_not(first)))
    return valid, rel.astype(F32)


def _dil_window(cur_ref, prev_ref, i, cols):
    if i > 0:
        return cur_ref[(i - 1) * DIL_W:(i + 1) * DIL_W, cols]
    return jnp.concatenate([prev_ref[:, cols], cur_ref[:DIL_W, cols]], axis=0)


CHUNK = 4 * DIL_W


def _dil_rows(block, S, dilation):
    start = _dil_start(block, S, dilation)
    return pl.ds(start, DIL_W, stride=dilation) if dilation > 1 else pl.ds(start, DIL_W)


def SPLIT(S):
    return (DIL_OUT // LANES, S, LANES)


def _dil_fwd(qkv, group, *, name):
    S = qkv.shape[1]
    dilation = DIL_PAIRS[group][1]
    bps = (S // dilation) // DIL_W
    slopes = _dil_slopes(group)
    nt = (((1,), (1,)), ((), ()))

    def body(q_ref, k_ref, v_ref, kp_ref, vp_ref, on_ref, ln_ref, o_ref, l_ref):
        n = pl.program_id(0)
        for i in range(4):
            valid, rel = _dil_tiles(i, n, bps)
            rows = slice(i * DIL_W, (i + 1) * DIL_W)
            for h in range(DIL_HEADS):
                cols = slice(h * HEAD_DIM, (h + 1) * HEAD_DIM)
                qh = q_ref[rows, cols]
                k2, v2 = _dil_window(k_ref, kp_ref, i, cols), _dil_window(v_ref, vp_ref, i, cols)
                s = lax.dot_general(qh, k2, nt, preferred_element_type=F32) * ATTN_SCALE - (slopes[h] * dilation) * rel
                s = jnp.where(valid, s, NEG_INF)
                m = jnp.max(s, axis=-1, keepdims=True)
                p = jnp.exp(s - m)
                den = jnp.sum(p, axis=-1, keepdims=True)
                acc = jnp.dot(p.astype(_CD), v2, preferred_element_type=F32)
                o_ref[rows, cols] = acc / den
                l_ref[rows, cols] = jnp.broadcast_to(m + jnp.log(den), (DIL_W, HEAD_DIM))
        for i in range(4):
            rows = slice(i * DIL_W, (i + 1) * DIL_W)
            nat = _dil_rows(4 * n + i, S, dilation)
            for half in range(DIL_OUT // LANES):
                cols = slice(half * LANES, (half + 1) * LANES)
                on_ref[half, nat, :] = o_ref[rows, cols]
                ln_ref[half, nat, :] = l_ref[rows, cols]

    def cur(which):
        return pl.BlockSpec((None, CHUNK, DIL_OUT), lambda n: (which, n, 0))

    def prev(which):
        return pl.BlockSpec((None, DIL_W, DIL_OUT), lambda n: (which, jnp.maximum(4 * n - 1, 0), 0))

    whole = pl.BlockSpec(SPLIT(S), lambda n: (0, 0, 0))
    return _pcall(body, name=name, grid=(S // CHUNK,), in_specs=[cur(0), cur(1), cur(2), prev(1), prev(2)],
                  out_specs=[whole, whole],
                  out_shape=[jax.ShapeDtypeStruct(SPLIT(S), F32), jax.ShapeDtypeStruct(SPLIT(S), F32)],
                  scratch_shapes=[pltpu.VMEM((CHUNK, DIL_OUT), F32), pltpu.VMEM((CHUNK, DIL_OUT), F32)],
                  compiler_params=_params("arbitrary"))(qkv, qkv, qkv, qkv, qkv)


STAT_OFFSET = HEAD_DIM // 2


def _dil_bwd(qkv, stats, do, group, *, name):
    S = qkv.shape[1]
    dilation = DIL_PAIRS[group][1]
    bps = (S // dilation) // DIL_W
    slopes = _dil_slopes(group)
    nchunk = S // CHUNK
    nt = (((1,), (1,)), ((), ()))
    tn = (((0,), (0,)), ((), ()))

    def body(q_ref, k_ref, v_ref, kp_ref, vp_ref, ln_ref, don_ref, dqn_ref, dkn_ref, dvn_ref,
             dk_s, dv_s, l_ref, do_ref, dq_ref):
        step = pl.program_id(0)
        n = nchunk - 1 - step
        for i in range(4):
            rows = slice(i * DIL_W, (i + 1) * DIL_W)
            nat = _dil_rows(4 * n + i, S, dilation)
            for half in range(DIL_OUT // LANES):
                cols = slice(half * LANES, (half + 1) * LANES)
                l_ref[rows, cols] = ln_ref[half, nat, :]
                do_ref[rows, cols] = don_ref[half, nat, :]

        @pl.when(step == 0)
        def _():
            dk_s[:, CHUNK:] = jnp.zeros((DIL_OUT, DIL_W), F32)
            dv_s[:, CHUNK:] = jnp.zeros((DIL_OUT, DIL_W), F32)

        dk_s[:, :CHUNK] = jnp.zeros((DIL_OUT, CHUNK), F32)
        dv_s[:, :CHUNK] = jnp.zeros((DIL_OUT, CHUNK), F32)
        for i in range(4):
            valid, rel = _dil_tiles(i, n, bps)
            rows = slice(i * DIL_W, (i + 1) * DIL_W)
            window = slice(i * DIL_W, (i + 2) * DIL_W)
            for h in range(DIL_HEADS):
                cols = slice(h * HEAD_DIM, (h + 1) * HEAD_DIM)
                qh = q_ref[rows, cols]
                k2, v2 = _dil_window(k_ref, kp_ref, i, cols), _dil_window(v_ref, vp_ref, i, cols)
                lh = l_ref[rows, h * HEAD_DIM:h * HEAD_DIM + 1]
                shift = l_ref[rows, h * HEAD_DIM + STAT_OFFSET:h * HEAD_DIM + STAT_OFFSET + 1]
                s = lax.dot_general(qh, k2, nt, preferred_element_type=F32) * ATTN_SCALE - (slopes[h] * dilation) * rel
                p = jnp.exp(jnp.where(valid, s, NEG_INF) - lh)
                dob = do_ref[rows, cols].astype(_CD)
                ds = p * (lax.dot_general(dob, v2, nt, preferred_element_type=F32) + shift)
                dsb = (ds * ATTN_SCALE).astype(_CD)
                dq_ref[rows, cols] = jnp.dot(dsb, k2, preferred_element_type=F32)
                dk_s[cols, window] += lax.dot_general(qh, dsb, tn, preferred_element_type=F32)
                dv_s[cols, window] += lax.dot_general(dob, p.astype(_CD), tn, preferred_element_type=F32)
        for i in range(4):
            rows = slice(i * DIL_W, (i + 1) * DIL_W)
            done = slice((i + 1) * DIL_W, (i + 2) * DIL_W)
            nat = _dil_rows(4 * n + i, S, dilation)
            dkb, dvb = dk_s[:, done].T, dv_s[:, done].T
            for half in range(DIL_OUT // LANES):
                cols = slice(half * LANES, (half + 1) * LANES)
                dqn_ref[half, nat, :] = dq_ref[rows, cols]
                dkn_ref[half, nat, :] = dkb[:, cols]
                dvn_ref[half, nat, :] = dvb[:, cols]
        dk_s[:, CHUNK:] = dk_s[:, :DIL_W]
        dv_s[:, CHUNK:] = dv_s[:, :DIL_W]

    def cur(which):
        return pl.BlockSpec((None, CHUNK, DIL_OUT), lambda s: (which, nchunk - 1 - s, 0))

    def prev(which):
        return pl.BlockSpec((None, DIL_W, DIL_OUT), lambda s: (which, jnp.maximum(4 * (nchunk - 1 - s) - 1, 0), 0))

    whole = pl.BlockSpec(SPLIT(S), lambda s: (0, 0, 0))
    shp = jax.ShapeDtypeStruct(SPLIT(S), F32)
    tile = pltpu.VMEM((CHUNK, DIL_OUT), F32)
    return _pcall(body, name=name, grid=(nchunk,),
                  in_specs=[cur(0), cur(1), cur(2), prev(1), prev(2), whole, whole],
                  out_specs=[whole, whole, whole], out_shape=[shp, shp, shp],
                  scratch_shapes=[pltpu.VMEM((DIL_OUT, CHUNK + DIL_W), F32), pltpu.VMEM((DIL_OUT, CHUNK + DIL_W), F32),
                                  tile, tile, tile],
                  compiler_params=pltpu.CompilerParams(dimension_semantics=("arbitrary",),
                                                       vmem_limit_bytes=VMEM_LIMIT_RESIDENT))(
        qkv, qkv, qkv, qkv, qkv, stats, do)


def _dil_mix_fwd(os_, ls_, *, name, tm=512):
    nh, S, _ = os_[0].shape

    def body(o0, o1, o2, l0, l1, l2, out_ref):
        for half in range(nh):
            ls = [l0[half], l1[half], l2[half]]
            m = jnp.maximum(jnp.maximum(ls[0], ls[1]), ls[2])
            es = [jnp.exp(l - m) for l in ls]
            den = es[0] + es[1] + es[2]
            mixed = (es[0] * o0[half] + es[1] * o1[half] + es[2] * o2[half]) / den
            out_ref[:, half * LANES:(half + 1) * LANES] = mixed.astype(out_ref.dtype)

    halves = pl.BlockSpec((nh, tm, LANES), lambda i: (0, i, 0))
    row = pl.BlockSpec((tm, nh * LANES), lambda i: (i, 0))
    return _pcall(body, name=name, grid=(S // tm,), in_specs=[halves] * 6, out_specs=row,
                  out_shape=jax.ShapeDtypeStruct((S, nh * LANES), _CD), compiler_params=_params("parallel"))(*os_, *ls_)


def _dil_mix_bwd(doa, os_, ls_, *, name, tm=512, after=None):
    nh, S, _ = os_[0].shape

    def body(d_ref, o0, o1, o2, l0, l1, l2, do0, do1, do2, st0, st1, st2):
        first = lax.broadcasted_iota(jnp.int32, (tm, LANES), 1) % HEAD_DIM < STAT_OFFSET
        for half in range(nh):
            dv = d_ref[:, half * LANES:(half + 1) * LANES]
            ls = [l0[half], l1[half], l2[half]]
            m = jnp.maximum(jnp.maximum(ls[0], ls[1]), ls[2])
            es = [jnp.exp(l - m) for l in ls]
            den = es[0] + es[1] + es[2]
            al = [e / den for e in es]
            da = [_head_sum(dv * o[half]) for o in (o0, o1, o2)]
            mean = al[0] * da[0] + al[1] * da[1] + al[2] * da[2]
            for a, l, do_ref, st_ref in zip(al, ls, (do0, do1, do2), (st0, st1, st2)):
                do_ref[half] = a * dv
                st_ref[half] = jnp.where(first, l, -a * mean)

    halves = pl.BlockSpec((nh, tm, LANES), lambda i: (0, i, 0))
    row = pl.BlockSpec((tm, nh * LANES), lambda i: (i, 0))
    shp = jax.ShapeDtypeStruct((nh, S, LANES), F32)
    return _pcall(body, after, name=name, grid=(S // tm,), in_specs=[row] + [halves] * 6, out_specs=[halves] * 6,
                  out_shape=[shp] * 6, compiler_params=_params("parallel"))(doa, *os_, *ls_)


FOX_T = 512


PACK = 2 * HEAD_DIM
HEAD_PAIRS = N_FOX_HEADS // 2
FOX_HPS = 8
Q_BLOCK0 = 0
K_BLOCK0 = FOX_WIDTH // PACK
V_BLOCK0 = 2 * FOX_WIDTH // PACK


def _pieces(x):
    hi = x.astype(jnp.bfloat16).astype(F32)
    r = x - hi
    mid = r.astype(jnp.bfloat16).astype(F32)
    lo = (r - mid).astype(jnp.bfloat16).astype(F32)
    return [hi, mid, lo]


def _extras(first, second, rows):
    lane = lax.broadcasted_iota(jnp.int32, (rows, HEAD_DIM), 1)
    out = jnp.zeros((rows, HEAD_DIM), F32)
    for base, triple in ((0, first), (3, second)):
        if all(isinstance(v, float) for v in triple) and len(set(triple)) == 1:
            if triple[0] != 0.0:
                out = jnp.where(jnp.logical_and(lane >= base, lane < base + 3), triple[0], out)
        else:
            for idx, val in enumerate(triple):
                out = jnp.where(lane == base + idx, val, out)
    return out


def _head_column(c, h):
    lane = lax.broadcasted_iota(jnp.int32, c.shape, 1)
    return jnp.sum(jnp.where(lane == h, c, 0.0), axis=1, keepdims=True)


ONES3 = [1.0, 1.0, 1.0]
ZEROS3 = [0.0, 0.0, 0.0]


def _fox_pack_fwd(qkv, c, *, name, tm=1024):
    S = qkv.shape[0]

    def body(q_ref, k_ref, v_ref, c_ref, qo_ref, ko_ref, vo_ref):
        hp = pl.program_id(1)
        cv = c_ref[...]
        v_extras = jnp.where(lax.broadcasted_iota(jnp.int32, (tm, HEAD_DIM), 1) < 3, 1.0, 0.0).astype(vo_ref.dtype)
        for hh in range(2):
            ch = _pieces(_head_column(cv, 2 * hp + hh))
            src = slice(hh * HEAD_DIM, (hh + 1) * HEAD_DIM)
            lo = slice(hh * PACK, hh * PACK + HEAD_DIM)
            hi = slice(hh * PACK + HEAD_DIM, (hh + 1) * PACK)
            qo_ref[:, lo] = (q_ref[:, src].astype(F32) * ATTN_SCALE).astype(qo_ref.dtype)
            qo_ref[:, hi] = _extras(ch, ONES3, tm).astype(qo_ref.dtype)
            ko_ref[:, lo] = k_ref[:, src]
            ko_ref[:, hi] = _extras(ONES3, [-p for p in ch], tm).astype(ko_ref.dtype)
            vo_ref[:, lo] = v_ref[:, src]
            vo_ref[:, hi] = v_extras

    def src(block0):
        return pl.BlockSpec((tm, PACK), lambda i, hp: (i, block0 + hp))

    out = pl.BlockSpec((tm, 2 * PACK), lambda i, hp: (i, hp))
    shp = jax.ShapeDtypeStruct((S, N_FOX_HEADS * PACK), _CD)
    return _pcall(body, name=name, grid=(S // tm, HEAD_PAIRS),
                  in_specs=[src(Q_BLOCK0), src(K_BLOCK0), src(V_BLOCK0), pl.BlockSpec((tm, PACK), lambda i, hp: (i, 0))],
                  out_specs=[out, out, out], out_shape=[shp, shp, shp],
                  compiler_params=_params("parallel", "parallel"))(qkv, qkv, qkv, c)


def _fox_fwd(qp, kp, vp, *, name):
    S = qp.shape[0]
    nt = S // FOX_T
    nt_dims = (((1,), (1,)), ((), ()))
    tn_dims = (((0,), (0,)), ((), ()))

    def body(i_tab, j_tab, q_ref, k_ref, v_ref, o_ref, l_ref, m_s, acc_s):
        t = pl.program_id(1)
        i, j = i_tab[t], j_tab[t]

        @pl.when(j == 0)
        def _():
            m_s[...] = jnp.full((FOX_HPS, 1, FOX_T), NEG_INF, F32)
            acc_s[...] = jnp.zeros((FOX_HPS, PACK, FOX_T), F32)

        def tile(diagonal):
            for hh in range(FOX_HPS):
                cols = slice(hh * PACK, (hh + 1) * PACK)
                st = lax.dot_general(k_ref[:, cols], q_ref[:, cols], nt_dims, preferred_element_type=F32)
                if diagonal:
                    key = lax.broadcasted_iota(jnp.int32, (FOX_T, FOX_T), 0)
                    qry = lax.broadcasted_iota(jnp.int32, (FOX_T, FOX_T), 1)
                    st = jnp.where(key <= qry, st, NEG_INF)
                m_old = m_s[hh]
                m_new = jnp.maximum(m_old, jnp.max(st, axis=0, keepdims=True))
                pt = jnp.exp(st - m_new)
                acc_s[hh] = jnp.exp(m_old - m_new) * acc_s[hh] + lax.dot_general(
                    v_ref[:, cols], pt.astype(_CD), tn_dims, preferred_element_type=F32)
                m_s[hh] = m_new

        @pl.when(j < i)
        def _():
            tile(False)

        @pl.when(j == i)
        def _():
            tile(True)
            for hh in range(FOX_HPS):
                acc = acc_s[hh]
                den = acc[HEAD_DIM:HEAD_DIM + 1, :]
                cols = slice(hh * HEAD_DIM, (hh + 1) * HEAD_DIM)
                o_ref[:, cols] = (acc[:HEAD_DIM, :] / den).T
                l_ref[:, cols] = jnp.broadcast_to(m_s[hh] + jnp.log(den), (HEAD_DIM, FOX_T)).T

    pairs = [(i, j) for i in range(nt) for j in range(i + 1)]
    i_tab = jnp.asarray([p[0] for p in pairs], jnp.int32)
    j_tab = jnp.asarray([p[1] for p in pairs], jnp.int32)
    qs = pl.BlockSpec((FOX_T, FOX_HPS * PACK), lambda hp, t, it, jt: (it[t], hp))
    ks = pl.BlockSpec((FOX_T, FOX_HPS * PACK), lambda hp, t, it, jt: (jt[t], hp))
    os_ = pl.BlockSpec((FOX_T, FOX_HPS * HEAD_DIM), lambda hp, t, it, jt: (it[t], hp))
    shp = jax.ShapeDtypeStruct((S, FOX_WIDTH), F32)
    grid_spec = pltpu.PrefetchScalarGridSpec(
        num_scalar_prefetch=2, grid=(N_FOX_HEADS // FOX_HPS, len(pairs)), in_specs=[qs, ks, ks], out_specs=[os_, os_],
        scratch_shapes=[pltpu.VMEM((FOX_HPS, 1, FOX_T), F32), pltpu.VMEM((FOX_HPS, PACK, FOX_T), F32)])
    return _pcall(body, name=name, grid_spec=grid_spec, out_shape=[shp, shp],
                  compiler_params=_params("parallel", "arbitrary"))(i_tab, j_tab, qp, kp, vp)


def _fox_pack_bwd(qkv, c, o, lse, do, *, name, tm=1024, after=None):
    S = qkv.shape[0]

    def body(q_ref, c_ref, o_ref, l_ref, do_ref, qo_ref, do_out_ref):
        hp = pl.program_id(1)
        cv = c_ref[...]
        for hh in range(2):
            src = slice(hh * HEAD_DIM, (hh + 1) * HEAD_DIM)
            lo = slice(hh * PACK, hh * PACK + HEAD_DIM)
            hi = slice(hh * PACK + HEAD_DIM, (hh + 1) * PACK)
            shift = _head_column(cv, 2 * hp + hh) - l_ref[:, hh * HEAD_DIM:hh * HEAD_DIM + 1]
            dov = do_ref[:, src]
            dsum = jnp.sum(dov * o_ref[:, src], axis=-1, keepdims=True)
            qo_ref[:, lo] = (q_ref[:, src].astype(F32) * ATTN_SCALE).astype(qo_ref.dtype)
            qo_ref[:, hi] = _extras(_pieces(shift), ONES3, tm).astype(qo_ref.dtype)
            do_out_ref[:, lo] = dov.astype(do_out_ref.dtype)
            do_out_ref[:, hi] = _extras(_pieces(-dsum), ZEROS3, tm).astype(do_out_ref.dtype)

    pair = pl.BlockSpec((tm, PACK), lambda i, hp: (i, hp))
    out = pl.BlockSpec((tm, 2 * PACK), lambda i, hp: (i, hp))
    shp = jax.ShapeDtypeStruct((S, N_FOX_HEADS * PACK), _CD)
    return _pcall(body, after, name=name, grid=(S // tm, HEAD_PAIRS),
                  in_specs=[pl.BlockSpec((tm, PACK), lambda i, hp: (i, Q_BLOCK0 + hp)),
                            pl.BlockSpec((tm, PACK), lambda i, hp: (i, 0)), pair, pair, pair],
                  out_specs=[out, out], out_shape=[shp, shp],
                  compiler_params=_params("parallel", "parallel"))(qkv, c, o, lse, do)


def _fox_bwd(qp, kp, vp, dop, *, name):
    S = qp.shape[0]
    nt = S // FOX_T
    nt_dims = (((1,), (1,)), ((), ()))
    tn_dims = (((0,), (0,)), ((), ()))

    def body(i_tab, j_tab, q_ref, k_ref, v_ref, do_ref, dq_ref, dk_ref, dv_ref, dc_ref, dr_ref,
             dq_s, dk_s, dv_s, dc_s, dr_s):
        t = pl.program_id(1)
        i, j = i_tab[t], j_tab[t]

        @pl.when(t == 0)
        def _():
            dq_s[...] = jnp.zeros((S, FOX_HPS * PACK), F32)
            dr_s[...] = jnp.zeros((FOX_HPS, 1, S), F32)

        @pl.when(i == j)
        def _():
            dk_s[...] = jnp.zeros((FOX_T, FOX_HPS * PACK), F32)
            dv_s[...] = jnp.zeros((FOX_T, FOX_HPS * PACK), F32)
            dc_s[...] = jnp.zeros((FOX_HPS, FOX_T, 1), F32)

        def tile(diagonal):
            rows = pl.ds(pl.multiple_of(i * FOX_T, FOX_T), FOX_T)
            for hh in range(FOX_HPS):
                cols = slice(hh * PACK, (hh + 1) * PACK)
                qv, kv, vv, dov = q_ref[:, cols], k_ref[:, cols], v_ref[:, cols], do_ref[:, cols]
                pt = jnp.exp(lax.dot_general(kv, qv, nt_dims, preferred_element_type=F32))
                if diagonal:
                    key = lax.broadcasted_iota(jnp.int32, (FOX_T, FOX_T), 0)
                    qry = lax.broadcasted_iota(jnp.int32, (FOX_T, FOX_T), 1)
                    pt = jnp.where(key <= qry, pt, 0.0)
                dst = pt * lax.dot_general(vv, dov, nt_dims, preferred_element_type=F32)
                dsb = dst.astype(_CD)
                dc_s[hh] += jnp.sum(dst, axis=1, keepdims=True)
                dr_s[hh, :, rows] += jnp.sum(dst, axis=0, keepdims=True)
                dv_s[:, cols] += jnp.dot(pt.astype(_CD), dov, preferred_element_type=F32)
                dk_s[:, cols] += jnp.dot(dsb, qv, preferred_element_type=F32)
                dq_s[rows, cols] += lax.dot_general(dsb, kv, tn_dims, preferred_element_type=F32)

        @pl.when(i > j)
        def _():
            tile(False)

        @pl.when(i == j)
        def _():
            tile(True)

        @pl.when(i == nt - 1)
        def _():
            for hh in range(FOX_HPS):
                src = slice(hh * PACK, hh * PACK + HEAD_DIM)
                dst_cols = slice(hh * HEAD_DIM, (hh + 1) * HEAD_DIM)
                dk_ref[:, dst_cols] = dk_s[:, src].astype(dk_ref.dtype)
                dv_ref[:, dst_cols] = dv_s[:, src].astype(dv_ref.dtype)
                dc_ref[:, dst_cols] = jnp.broadcast_to(dc_s[hh], (FOX_T, HEAD_DIM))

        @pl.when(t == len(pairs) - 1)
        def _():
            for hh in range(FOX_HPS):
                dq_ref[:, hh * HEAD_DIM:(hh + 1) * HEAD_DIM] = (
                    dq_s[:, hh * PACK:hh * PACK + HEAD_DIM] * ATTN_SCALE).astype(dq_ref.dtype)
            dr_ref[...] = dr_s[...]

    pairs = [(i, j) for j in range(nt) for i in range(j, nt)]
    i_tab = jnp.asarray([p[0] for p in pairs], jnp.int32)
    j_tab = jnp.asarray([p[1] for p in pairs], jnp.int32)
    wide, narrow = FOX_HPS * PACK, FOX_HPS * HEAD_DIM
    qs = pl.BlockSpec((FOX_T, wide), lambda hp, t, it, jt: (it[t], hp))
    ks = pl.BlockSpec((FOX_T, wide), lambda hp, t, it, jt: (jt[t], hp))
    whole = pl.BlockSpec((S, narrow), lambda hp, t, it, jt: (0, hp))
    cs = pl.BlockSpec((FOX_T, narrow), lambda hp, t, it, jt: (jt[t], hp))
    rs = pl.BlockSpec((FOX_HPS, 1, S), lambda hp, t, it, jt: (hp, 0, 0))
    shp = jax.ShapeDtypeStruct((S, FOX_WIDTH), _CD)
    grid_spec = pltpu.PrefetchScalarGridSpec(
        num_scalar_prefetch=2, grid=(N_FOX_HEADS // FOX_HPS, len(pairs)), in_specs=[qs, ks, ks, qs],
        out_specs=[whole, cs, cs, cs, rs],
        scratch_shapes=[pltpu.VMEM((S, wide), F32), pltpu.VMEM((FOX_T, wide), F32),
                        pltpu.VMEM((FOX_T, wide), F32), pltpu.VMEM((FOX_HPS, FOX_T, 1), F32),
                        pltpu.VMEM((FOX_HPS, 1, S), F32)])
    return _pcall(body, name=name, grid_spec=grid_spec,
                  out_shape=[shp, shp, shp, jax.ShapeDtypeStruct((S, FOX_WIDTH), F32),
                             jax.ShapeDtypeStruct((N_FOX_HEADS, 1, S), F32)],
                  compiler_params=_params("parallel", "arbitrary"))(i_tab, j_tab, qp, kp, vp, dop)


def _layer_step(x, tgt, w, p, late_weights=None, grad_sink=None, after=None, first_weights=None):
    S = x.shape[0]
    after_norm, after_proj = after if after is not None else (None, None)
    h = _rms_fwd(x, p["norm_mix_g"], name="rms_mix", after=after_norm)
    if first_weights is not None:
        w = {**w, **first_weights(h)}
    qkv = _mm(h, w["qkv"][:, 3 * DIL_WIDTH:], name="proj_fox", out_dtype=_CD, tn=768, tm=2048, after=after_proj)
    dil_qkv = _proj_dil(h, w["qkv"], name="proj_dil")
    zf = _mm(h, w["f"], name="proj_f")
    gl = _mm(h, w["g"], name="proj_gate", tn=1024, out_dtype=_CD)

    dil_o, dil_l = [], []
    for g in range(N_DIL_GROUPS):
        og, lg = _dil_fwd(dil_qkv[g], g, name=f"dil_fwd{g}")
        dil_o.append(og), dil_l.append(lg)
    o_a = _dil_mix_fwd(dil_o, dil_l, name="dil_mix")

    c = _fox_cumsum(zf, p["b_fgt"], name="fox_cumsum")
    fqp, fkp, fvp = _fox_pack_fwd(qkv, c, name="fox_pack")
    o_b, flse = _fox_fwd(fqp, fkp, fvp, name="fox_fwd")

    if late_weights is not None:
        w = {**w, **late_weights(o_b)}
    y_a = _mm(o_a, w["dil_out"], name="y_a", tn=1024, out_dtype=_CD)
    y_b = _mm(o_b, w["fox_out"], name="y_b", tn=1024, out_dtype=_CD)
    merged, x1, h2 = _gated_mix_out(gl, p["b_gate"], y_a, y_b, w["out"], x, p["norm_ffn_g"], name="mix_out")
    gate, up, act = _ffn_in_act(h2, w["ffn_in"], name="ffn_in")
    loss, dx2, dg_final = _ffn_down_loss(act, w["ffn_down"], x1, p["norm_final_g"], tgt, name="ffn_down_loss")

    gw_ffn_down = _mm(act, dx2, name="gw_ffn_down", ta=True, out_dtype=_CD, tm=1408)
    dgu = _d_swiglu(dx2, w["ffn_down"], gate, up, name="d_swiglu")
    gw_ffn_in = _mm(h2, dgu, name="gw_ffn_in", ta=True, out_dtype=_CD, tn=1408, out_blocks=1408, b_halves=True)
    sink = grad_sink if grad_sink is not None else (lambda group, grads: None)
    tok = sink("ffn", dict(ffn_in=gw_ffn_in, ffn_down=gw_ffn_down))
    dx1, dg_ffn = _mm(dgu, w["ffn_in"], name="d_h2", tb=True, tk=1408, b_blocks=True, tm=1024, a_halves=True,
                      rms_bwd=(x1, p["norm_ffn_g"], dx2), after=tok)

    gw_out = _mm(merged, dx1, name="gw_out", ta=True, out_dtype=_CD)
    dy_a, dy_b, dgl, db_gate = _gate_bwd(dx1, w["out"], gl, p["b_gate"], y_a, y_b, name="gate_bwd")
    do_a = _mm(dy_a, w["dil_out"], name="d_o_a", tb=True)
    gw_dil_out = _mm(o_a, dy_a, name="gw_dil_out", ta=True, out_dtype=_CD, tn=1024)
    do_b = _mm(dy_b, w["fox_out"], name="d_o_b", tb=True)
    gw_fox_out = _mm(o_b, dy_b, name="gw_fox_out", ta=True, out_dtype=_CD, tn=1024)
    tok = sink("mix", dict(dil_out=gw_dil_out, fox_out=gw_fox_out, out=gw_out))

    bqp, bdop = _fox_pack_bwd(qkv, c, o_b, flse, do_b, name="fox_pack_bwd", after=tok)
    dqp, dkp, dvp, dck, dcq = _fox_bwd(bqp, fkp, fvp, bdop, name="fox_bwd")
    dc = dcq[:, 0, :].T - dck.reshape(S, N_FOX_HEADS, HEAD_DIM)[:, :, 0]
    dc = jnp.pad(dc, ((0, 0), (0, F_PAD - N_FOX_HEADS)))
    dzf, db_fgt = _fox_cumsum_bwd(dc, zf, p["b_fgt"], name="fox_cumsum_bwd")

    douts = _dil_mix_bwd(do_a, dil_o, dil_l, name="dil_mix_bwd", after=tok)
    dqs, dks, dvs = [], [], []
    for g in range(N_DIL_GROUPS):
        dq, dk, dv = _dil_bwd(dil_qkv[g], douts[3 + g], douts[g], g, name=f"dil_bwd{g}")
        for parts, t in ((dqs, dq), (dks, dk), (dvs, dv)):
            parts.extend([t[0].astype(_CD), t[1].astype(_CD)])
    dqkv = jnp.concatenate(dqs + dks + dvs + [dqp, dkp, dvp], axis=1)

    gw_qkv = _mm(h, dqkv, name="gw_qkv", ta=True, out_dtype=_CD, tn=768)
    gw_g = _mm(h, dgl, name="gw_gate", ta=True, out_dtype=_CD)
    gw_f = _mm(h, dzf, name="gw_f", ta=True, out_dtype=_CD)
    tok = sink("in", dict(qkv=gw_qkv, f=gw_f, g=gw_g))
    dx, dg_mix = _mm(dqkv, w["qkv"], name="d_h", tb=True, tk=QKV_COLS, tm=256, more=((dgl, w["g"]), (dzf, w["f"])),
                     rms_bwd=(x, p["norm_mix_g"], dx1), after=tok)

    gw = dict(qkv=gw_qkv, f=gw_f, g=gw_g, dil_out=gw_dil_out, fox_out=gw_fox_out, out=gw_out, ffn_in=gw_ffn_in,
              ffn_down=gw_ffn_down)
    small = dict(norm_mix_g=dg_mix, b_fgt=db_fgt, b_gate=db_gate, norm_ffn_g=dg_ffn, norm_final_g=dg_final)
    return loss, dx, gw, small


def _position():
    return lax.axis_index("x"), lax.axis_index("y"), lax.axis_index("c")


def _other_chips(x, y):
    return [(1 - x, y), (x, 1 - y), (1 - x, 1 - y)]


ROW_TILE = 16


def _row_chunks(rows, want=4):
    n = want
    while n > 1 and rows % (n * ROW_TILE):
        n //= 2
    return n


SEM_SPEC = pl.BlockSpec(memory_space=pltpu.SEMAPHORE)
ANY_SPEC = pl.BlockSpec(memory_space=pl.ANY)
DATAFLOW = pltpu.SideEffectType.DATAFLOW_SIDE_EFFECTING


def _in_hbm(a):
    return pltpu.with_memory_space_constraint(a, pltpu.HBM)


def _split_copy_start(srcs, land_shapes, copies, after, *, name):
    n, m = len(srcs), len(land_shapes)

    def body(*refs):
        src_refs, land_refs = refs[:n], refs[n:n + m]
        send_sems, recv_sems = refs[n + m + 1], refs[n + m + 2]
        token = refs[-1]
        x, y, c = _position()
        for k, (src, dst, peer) in enumerate(copies(x, y, c, src_refs, land_refs)):
            pltpu.make_async_remote_copy(src_ref=src, dst_ref=dst, send_sem=send_sems.at[k], recv_sem=recv_sems.at[k],
                                         device_id=peer, device_id_type=MESH).start()
        token[...] = jnp.zeros_like(token)

    lands = [lax.empty(s.shape, s.dtype) for s in land_shapes]
    count = len(copies(0, 0, 0, srcs, lands))
    out = _pcall(
        body, name=name,
        out_shape=(pltpu.SemaphoreType.DMA((count,)), pltpu.SemaphoreType.DMA((count,)),
                   *[pltpu.HBM(s.shape, s.dtype) for s in srcs], *[pltpu.HBM(s.shape, s.dtype) for s in land_shapes],
                   jax.ShapeDtypeStruct((8, 128), F32)),
        in_specs=[HBM_SPEC] * (n + m) + [ANY_SPEC],
        out_specs=(SEM_SPEC, SEM_SPEC, *[HBM_SPEC] * (n + m), pl.BlockSpec(memory_space=pltpu.VMEM)),
        input_output_aliases={k: 2 + k for k in range(n + m)},
        compiler_params=pltpu.CompilerParams(has_side_effects=DATAFLOW),
    )(*[_in_hbm(s) for s in srcs], *[_in_hbm(l) for l in lands], after)
    return out[0], out[1], list(out[2:2 + n]), list(out[2 + n:2 + n + m]), out[-1]


def _split_copy_wait(send_sems, recv_sems, srcs, lands, copies, after, *, name):
    n, m = len(srcs), len(lands)

    def body(*refs):
        src_refs, land_refs = refs[:n], refs[n:n + m]
        send, recv = refs[n + m], refs[n + m + 1]
        x, y, c = _position()
        for k, (src, dst, peer) in enumerate(copies(x, y, c, src_refs, land_refs)):
            cp = pltpu.make_async_remote_copy(src_ref=src, dst_ref=dst, send_sem=send.at[k], recv_sem=recv.at[k],
                                              device_id=peer, device_id_type=MESH)
            cp.wait_send()
            cp.wait_recv()

    afters = list(after) if isinstance(after, (list, tuple)) else [after]
    out = _pcall(
        body, name=name,
        out_shape=tuple(pltpu.HBM(s.shape, s.dtype) for s in list(srcs) + list(lands)),
        in_specs=[HBM_SPEC] * (n + m) + [SEM_SPEC, SEM_SPEC] + [ANY_SPEC] * len(afters),
        out_specs=tuple([HBM_SPEC] * (n + m)),
        input_output_aliases={k: k for k in range(n + m)},
        compiler_params=pltpu.CompilerParams(has_side_effects=DATAFLOW),
    )(*srcs, *lands, send_sems, recv_sems, *afters)
    return list(out[:n]), list(out[n:])


def _gather_copies(x, y, c, shard_refs, land_refs):
    out = []
    for s, l in zip(shard_refs, land_refs):
        half = s.shape[0] // 2
        nq = _row_chunks(half)
        for cx, cy in _other_chips(x, y):
            for q in range(nq):
                rows = pl.ds(c * half + q * (half // nq), half // nq)
                out.append((s.at[rows, :], l.at[2 * x + y, rows, :], (cx, cy, c)))
    return out


def _gather_whole_copies(x, y, c, shard_refs, land_refs):
    out = []
    for s, l in zip(shard_refs, land_refs):
        nq = _row_chunks(s.shape[0])
        for cx, cy in _other_chips(x, y):
            for q in range(nq):
                rows = pl.ds(q * (s.shape[0] // nq), s.shape[0] // nq)
                out.append((s.at[rows, :], l.at[2 * x + y, rows, :], (cx, cy, c)))
    return out


def _scatter_all_copies(x, y, c, block_refs, land_refs):
    out = []
    for g, l in zip(block_refs, land_refs):
        half = g.shape[1] // 2
        nq = _row_chunks(half)
        size = half // nq
        for q in range(nq):
            rows = pl.ds((1 - c) * half + q * size, size)
            out.append((g.at[2 * x + y, rows, :], l.at[0, pl.ds(q * size, size), :], (x, y, 1 - c)))
        for r, (cx, cy) in enumerate(_other_chips(x, y)):
            for j in range(2):
                h = c if j == 0 else 1 - c
                for q in range(nq):
                    rows = pl.ds(h * half + q * size, size)
                    out.append((g.at[2 * cx + cy, rows, :], l.at[1 + 2 * r + j, pl.ds(q * size, size), :], (cx, cy, h)))
    return out


def _forward_halves(lands, *, name):
    n = len(lands)

    def body(*refs):
        ins = refs[:n]
        send_sems, recv_sems = refs[2 * n:]
        x, y, c = _position()
        copies = []
        for w in range(n):
            half = ins[w].shape[1] // 2
            for r, (cx, cy) in enumerate(_other_chips(x, y)):
                blk = ins[w].at[2 * cx + cy, pl.ds(c * half, half), :]
                cp = pltpu.make_async_remote_copy(src_ref=blk, dst_ref=blk, send_sem=send_sems.at[w, r],
                                                  recv_sem=recv_sems.at[w, r], device_id=(x, y, 1 - c),
                                                  device_id_type=MESH)
                cp.start()
                copies.append(cp)
        for w in range(n):
            half = ins[w].shape[1] // 2
            for r, (cx, cy) in enumerate(_other_chips(x, y)):
                blk = ins[w].at[2 * cx + cy, pl.ds((1 - c) * half, half), :]
                pltpu.make_async_remote_copy(src_ref=blk, dst_ref=blk, send_sem=send_sems.at[w, r],
                                             recv_sem=recv_sems.at[w, r], device_id=(x, y, 1 - c),
                                             device_id_type=MESH).wait_recv()
        for cp in copies:
            cp.wait_send()

    return _pcall(
        body, name=name, in_specs=[HBM_SPEC] * n, out_specs=[HBM_SPEC] * n,
        out_shape=[jax.ShapeDtypeStruct(l.shape, l.dtype) for l in lands],
        input_output_aliases={k: k for k in range(n)},
        scratch_shapes=[pltpu.SemaphoreType.DMA((n, 3)), pltpu.SemaphoreType.DMA((n, 3))],
    )(*lands)


def _share_halves(halves, *, name):
    n = len(halves)

    def body(*refs):
        ins, outs = refs[:n], refs[n:2 * n]
        send_sems, recv_sems = refs[2 * n:]
        x, y, c = _position()
        copies = []
        for w in range(n):
            cp = pltpu.make_async_remote_copy(src_ref=ins[w], dst_ref=outs[w], send_sem=send_sems.at[w],
                                              recv_sem=recv_sems.at[w], device_id=(x, y, 1 - c), device_id_type=MESH)
            cp.start()
            copies.append(cp)
        for cp in copies:
            cp.wait()

    return _pcall(
        body, name=name, in_specs=[HBM_SPEC] * n, out_specs=[HBM_SPEC] * n,
        out_shape=[jax.ShapeDtypeStruct(h.shape, h.dtype) for h in halves],
        scratch_shapes=[pltpu.SemaphoreType.DMA((n,)), pltpu.SemaphoreType.DMA((n,))],
    )(*halves)


def _sum_small(part, after=None):
    rows, width = part.shape

    def body(x_ref, out_ref, all_ref, send_sems, recv_sems):
        x, y, c = _position()
        me, sibling = (x, y, c), (x, y, 1 - c)
        chips = _other_chips(x, y)

        def block(px, py, pc):
            return all_ref.at[pl.ds((4 * px + 2 * py + pc) * rows, rows), :]

        def copy(k, blk, to, src=None):
            return pltpu.make_async_remote_copy(
                src_ref=block(*blk) if src is None else src, dst_ref=block(*blk), send_sem=send_sems.at[k],
                recv_sem=recv_sems.at[k], device_id=to, device_id_type=MESH)

        all_ref[pl.ds((4 * x + 2 * y + c) * rows, rows), :] = x_ref[...]
        first = [copy(0, me, sibling, src=x_ref)]
        first += [copy(1 + j, me, (*chip, c), src=x_ref) for j, chip in enumerate(chips)]
        for cp in first:
            cp.start()
        passed = [copy(4 + j, (*chip, c), sibling) for j, chip in enumerate(chips)]
        for j, chip in enumerate(chips):
            copy(1 + j, (*chip, c), me).wait_recv()
            passed[j].start()
        copy(0, sibling, me).wait_recv()
        for j, chip in enumerate(chips):
            copy(4 + j, (*chip, 1 - c), me).wait_recv()
        for cp in first + passed:
            cp.wait_send()
        total = all_ref[0:rows, :]
        for d in range(1, 8):
            total = total + all_ref[d * rows:(d + 1) * rows, :]
        out_ref[...] = total

    vm = pl.BlockSpec(memory_space=pltpu.VMEM)
    return _pcall(
        body, after, name="sum_small", in_specs=[vm], out_specs=vm, out_shape=jax.ShapeDtypeStruct((rows, width), F32),
        scratch_shapes=[pltpu.VMEM((8 * rows, width), F32), pltpu.SemaphoreType.DMA((7,)), pltpu.SemaphoreType.DMA((7,))],
    )(part)


def _row_tile(R, C, itemsize=4, budget=1 << 20):
    for t in (512, 256, 128, 64, 32, 16, 8):
        if R % t == 0 and t * C * itemsize <= budget:
            return t
    return R


def _add_all(g, recv, where, *, name):
    _, R, C = g.shape
    half = R // 2
    t = _row_tile(half, C)
    nb = half // t

    def body(w_ref, g_ref, r_ref, o_ref):
        total = g_ref[0].astype(F32)
        for k in range(7):
            total = total + r_ref[k].astype(F32)
        o_ref[...] = total

    grid_spec = pltpu.PrefetchScalarGridSpec(
        num_scalar_prefetch=1, grid=(nb,),
        in_specs=[pl.BlockSpec((1, t, C), lambda i, wr: (wr[0], wr[1] * nb + i, 0)),
                  pl.BlockSpec((7, t, C), lambda i, wr: (0, i, 0))],
        out_specs=pl.BlockSpec((t, C), lambda i, wr: (i, 0)))
    return _pcall(body, name=name, grid_spec=grid_spec, out_shape=jax.ShapeDtypeStruct((half, C), F32),
                  compiler_params=_params("parallel"))(where, g, recv)


def _adamw(w, g, m, v, *, name):
    R, C = w.shape
    t = _row_tile(R, C)
    c1 = 1.0 - ADAM_B1 ** ADAM_STEP
    c2 = 1.0 - ADAM_B2 ** ADAM_STEP

    def body(w_ref, g_ref, m_ref, v_ref, d_ref, nm_ref, nv_ref):
        gv = g_ref[...]
        mn = ADAM_B1 * m_ref[...] + (1.0 - ADAM_B1) * gv
        vn = ADAM_B2 * v_ref[...] + (1.0 - ADAM_B2) * (gv * gv)
        d_ref[...] = -ADAM_LR * ((mn / c1) / (jnp.sqrt(vn / c2) + ADAM_EPS) + ADAM_WD * w_ref[...])
        nm_ref[...] = mn
        nv_ref[...] = vn

    blk = pl.BlockSpec((t, C), lambda i: (i, 0))
    shp = jax.ShapeDtypeStruct((R, C), F32)
    return _pcall(body, name=name, grid=(R // t,), in_specs=[blk] * 4, out_specs=[blk] * 3, out_shape=[shp] * 3,
                  compiler_params=_params("parallel"))(w, g, m, v)


def _adamw_halves(w, mine, theirs, m, v, core, *, name, after=None):
    R, C = w.shape
    half = R // 2
    t = _row_tile(half, C)
    nbh = half // t
    c1 = 1.0 - ADAM_B1 ** ADAM_STEP
    c2 = 1.0 - ADAM_B2 ** ADAM_STEP

    def body(core_ref, w_ref, a_ref, b_ref, m_ref, v_ref, *rest):
        g_ref, d_ref, nm_ref, nv_ref = rest[-4:]
        gv = jnp.where(pl.program_id(0) // nbh == core_ref[0], a_ref[...], b_ref[...])
        mn = ADAM_B1 * m_ref[...] + (1.0 - ADAM_B1) * gv
        vn = ADAM_B2 * v_ref[...] + (1.0 - ADAM_B2) * (gv * gv)
        g_ref[...] = gv
        d_ref[...] = -ADAM_LR * ((mn / c1) / (jnp.sqrt(vn / c2) + ADAM_EPS) + ADAM_WD * w_ref[...])
        nm_ref[...] = mn
        nv_ref[...] = vn

    blk = pl.BlockSpec((t, C), lambda i, cr: (i, 0))
    hblk = pl.BlockSpec((t, C), lambda i, cr: (i % nbh, 0))
    shp = jax.ShapeDtypeStruct((R, C), F32)
    tied = [] if after is None else [after]
    grid_spec = pltpu.PrefetchScalarGridSpec(num_scalar_prefetch=1, grid=(2 * nbh,),
                                             in_specs=[blk, hblk, hblk, blk, blk] + [ANY_SPEC] * len(tied),
                                             out_specs=[blk] * 4)
    return _pcall(body, name=name, grid_spec=grid_spec, out_shape=[shp] * 4,
                  compiler_params=_params("parallel"))(core, w, mine, theirs, m, v, *tied)


BIG = ("w_in", "w_dil_out", "w_fox_out", "w_out", "w_ffn_in", "w_ffn_down")
SMALL = ("norm_mix_g", "b_fgt", "b_gate", "norm_ffn_g", "norm_final_g")
ORDER = ("norm_mix_g", "w_in", "b_fgt", "b_gate", "w_dil_out", "w_fox_out", "w_out", "norm_ffn_g", "w_ffn_in",
         "w_ffn_down", "norm_final_g")
SMALL_ROWS = {"norm_mix_g": (0, 1), "b_gate": (1, 3), "norm_ffn_g": (3, 4), "norm_final_g": (4, 5), "b_fgt": (5, 6)}


def _columns_to_blocks(full, ncol):
    K = full.shape[0]
    return full.reshape(K, 4, ncol).transpose(1, 0, 2)


def _pieces_to_blocks(pieces, ncol):
    spans, start = [], 0
    for piece in pieces:
        spans.append((piece, start, start + piece.shape[1]))
        start += piece.shape[1]
    assert start == 4 * ncol
    blocks = []
    for k in range(4):
        lo, hi = k * ncol, (k + 1) * ncol
        parts = [p[:, max(lo, a) - a:min(hi, b) - a] for p, a, b in spans if a < hi and b > lo]
        blocks.append(parts[0] if len(parts) == 1 else jnp.concatenate(parts, axis=1))
    return jnp.stack(blocks)


def _blocks_to_pieces(blocks, widths):
    n, K, ncol = blocks.shape
    assert sum(widths) == n * ncol
    pieces, lo = [], 0
    for width in widths:
        hi = lo + width
        parts = [blocks[k][:, max(lo, k * ncol) - k * ncol:min(hi, (k + 1) * ncol) - k * ncol]
                 for k in range(n) if k * ncol < hi and (k + 1) * ncol > lo]
        pieces.append(parts[0] if len(parts) == 1 else jnp.concatenate(parts, axis=1))
        lo = hi
    return pieces


def _blocks_to_columns(blocks):
    n, K, ncol = blocks.shape
    return blocks.transpose(1, 0, 2).reshape(K, n * ncol)


def kernel(x, norm_mix_g, w_in, b_fgt, b_gate, w_dil_out, w_fox_out, w_out, norm_ffn_g, w_ffn_in, w_ffn_down, norm_final_g, loss_target, m_norm_mix_g, m_w_in, m_b_fgt, m_b_gate, m_w_dil_out, m_w_fox_out, m_w_out, m_norm_ffn_g, m_w_ffn_in, m_w_ffn_down, m_norm_final_g, v_norm_mix_g, v_w_in, v_b_fgt, v_b_gate, v_w_dil_out, v_w_fox_out, v_w_out, v_norm_ffn_g, v_w_ffn_in, v_w_ffn_down, v_norm_final_g):
    weights = dict(norm_mix_g=norm_mix_g, w_in=w_in, b_fgt=b_fgt, b_gate=b_gate, w_dil_out=w_dil_out,
                   w_fox_out=w_fox_out, w_out=w_out, norm_ffn_g=norm_ffn_g, w_ffn_in=w_ffn_in, w_ffn_down=w_ffn_down,
                   norm_final_g=norm_final_g)
    m_in = dict(norm_mix_g=m_norm_mix_g, w_in=m_w_in, b_fgt=m_b_fgt, b_gate=m_b_gate, w_dil_out=m_w_dil_out,
                w_fox_out=m_w_fox_out, w_out=m_w_out, norm_ffn_g=m_norm_ffn_g, w_ffn_in=m_w_ffn_in,
                w_ffn_down=m_w_ffn_down, norm_final_g=m_norm_final_g)
    v_in = dict(norm_mix_g=v_norm_mix_g, w_in=v_w_in, b_fgt=v_b_fgt, b_gate=v_b_gate, w_dil_out=v_w_dil_out,
                w_fox_out=v_w_fox_out, w_out=v_w_out, norm_ffn_g=v_norm_ffn_g, w_ffn_in=v_w_ffn_in,
                w_ffn_down=v_w_ffn_down, norm_final_g=v_norm_final_g)
    c = lax.axis_index("c")
    chip = 2 * lax.axis_index("x") + lax.axis_index("y")

    shards = {n: weights[n][0].astype(_CD) for n in BIG}
    in_shape = jax.ShapeDtypeStruct((4,) + shards["w_in"].shape, _CD)
    send_i, recv_i, in_src, in_land, token_in = _split_copy_start(
        [shards["w_in"]], [in_shape], _gather_copies, norm_mix_g, name="gather_in_start")
    late = BIG[1:]
    send_g, recv_g, late_src, late_land, token = _split_copy_start(
        [shards[n] for n in late], [jax.ShapeDtypeStruct((4,) + shards[n].shape, _CD) for n in late],
        _gather_whole_copies, token_in, name="gather_late_start")
    adam_in = [t[0] + token_in[0, 0] for t in (w_in, m_w_in, v_w_in)]
    p = dict(norm_mix_g=norm_mix_g, b_fgt=jnp.pad(b_fgt, ((0, 0), (0, F_PAD - N_FOX_HEADS))), b_gate=b_gate,
             norm_ffn_g=norm_ffn_g, norm_final_g=norm_final_g.reshape(1, D_MODEL))

    def first_weights(after):
        own, lands = _split_copy_wait(send_i, recv_i, in_src, in_land, _gather_copies, [after] + adam_in,
                                      name="gather_in_wait")
        (g_in,) = _forward_halves(lands, name="gather_in_forward")
        blocks = lax.dynamic_update_index_in_dim(g_in, own[0], chip, 0)
        qkv, f, g = _blocks_to_pieces(blocks, (QKV_COLS, N_FOX_HEADS, 2 * D_MODEL))
        return dict(qkv=qkv, f=jnp.pad(f, ((0, 0), (0, F_PAD - N_FOX_HEADS))), g=g)

    def late_weights(after):
        own, lands = _split_copy_wait(send_g, recv_g, late_src, late_land, _gather_whole_copies, after,
                                      name="gather_late_wait")
        g_dil, g_fox, g_out, g_ffn_in, g_ffn_down = [
            lax.dynamic_update_index_in_dim(l, s, chip, 0) for l, s in zip(lands, own)]
        return dict(dil_out=_blocks_to_columns(g_dil), fox_out=_blocks_to_columns(g_fox),
                    out=g_out.reshape(D_MODEL, D_MODEL), ffn_in=g_ffn_in,
                    ffn_down=g_ffn_down.reshape(D_FF, D_MODEL))

    def to_blocks(n, full):
        shape = weights[n].shape
        if full.ndim == 3:
            return full
        if n in ("w_out", "w_ffn_down"):
            return full.reshape(4, shape[1], shape[2])
        return _columns_to_blocks(full, shape[2])

    in_flight = {}

    def grad_sink(group, gw):
        if group == "in":
            named = {"w_in": _pieces_to_blocks([gw["qkv"], gw["f"][:, :N_FOX_HEADS], gw["g"]], weights["w_in"].shape[2])}
        else:
            named = {"w_" + k: v for k, v in gw.items()}
        srcs = [to_blocks(n, named[n]) for n in named]
        lands = [jax.ShapeDtypeStruct((7, s.shape[1] // 2, s.shape[2]), s.dtype) for s in srcs]
        started = _split_copy_start(srcs, lands, _scatter_all_copies, next(iter(gw.values())),
                                    name=f"scatter_{group}_start")
        in_flight[group] = (list(named), started)
        return started[-1]

    loss_part, grad_x, gw, small = _layer_step(x[0], loss_target[0], {}, p, late_weights, grad_sink,
                                               (token_in, token), first_weights)

    where = jnp.stack([chip, c]).astype(jnp.int32)

    def summed_halves(groups, after, name):
        halves = {}
        for group in groups:
            names, (send_s, recv_s, srcs, lands, _) = in_flight[group]
            srcs, recv = _split_copy_wait(send_s, recv_s, srcs, lands, _scatter_all_copies, after,
                                          name=f"scatter_{group}_wait")
            halves.update({n: _add_all(s, r, where, name=f"add_all_{n}") for n, s, r in zip(names, srcs, recv)})
        return {n: (h, o) for (n, h), o in zip(halves.items(), _share_halves(list(halves.values()), name=name))}

    grad_halves = summed_halves(("ffn", "mix"), grad_x, "share_halves")

    out_g, out_d, out_m, out_v = {}, {}, {}, {}
    core = jnp.reshape(c, (1,)).astype(jnp.int32)

    def adamw_big(n, after=None):
        shape = weights[n].shape
        wmv = adam_in if n == "w_in" else [t[0] for t in (weights[n], m_in[n], v_in[n])]
        mine, theirs = grad_halves[n]
        outs = _adamw_halves(wmv[0], mine, theirs, wmv[1], wmv[2], core, name=f"adamw_{n}", after=after)
        out_g[n], out_d[n], out_m[n], out_v[n] = [t.reshape(shape) for t in outs]

    early = ("w_dil_out", "w_fox_out", "w_out", "w_ffn_down")
    for n in early:
        adamw_big(n)
    grad_halves.update(summed_halves(("in",), [grad_x] + [out_d[n] for n in early], "share_halves_in"))
    adamw_big("w_in")

    packed = jnp.concatenate([
        small["norm_mix_g"], small["b_gate"].reshape(2, D_MODEL), small["norm_ffn_g"], small["norm_final_g"],
        jnp.pad(small["b_fgt"], ((0, 0), (0, D_MODEL - F_PAD))), jnp.pad(loss_part, ((0, 0), (0, D_MODEL - 1))),
        jnp.zeros((1, D_MODEL), F32)], axis=0)
    summed = _sum_small(packed, after=out_d["w_in"])
    loss = summed[6, 0]
    adamw_big("w_ffn_in", after=summed)

    for n in SMALL:
        lo, hi = SMALL_ROWS[n]
        shape = weights[n].shape
        g2 = summed[lo:hi].reshape(1, -1)[:, :weights[n].size]
        d2, m2, v2 = _adamw(weights[n].reshape(g2.shape), g2, m_in[n].reshape(g2.shape), v_in[n].reshape(g2.shape),
                            name=f"adamw_{n}")
        out_g[n], out_d[n], out_m[n], out_v[n] = [t.reshape(shape) for t in (g2, d2, m2, v2)]
    return (loss, grad_x[None], *[out_g[n] for n in ORDER], *[out_d[n] for n in ORDER],
            *[out_m[n] for n in ORDER], *[out_v[n] for n in ORDER])
```

```python
import numpy as np
import jax
import jax.numpy as jnp
from jax import lax
from jax.experimental import pallas as pl
from jax.experimental.pallas import tpu as pltpu

F32 = jnp.float32
_CD = jnp.bfloat16

D_MODEL = 1024
HEAD_DIM = 64
DIL_PAIRS = ((128, 1), (512, 4), (2048, 16))
N_DIL_GROUPS = 3
DIL_HEADS = 4
DIL_W = 128
DIL_OUT = DIL_HEADS * HEAD_DIM
DIL_WIDTH = N_DIL_GROUPS * DIL_OUT
N_FOX_HEADS = 8
FOX_WIDTH = N_FOX_HEADS * HEAD_DIM
D_FF = 2816
QKV_COLS = 3 * DIL_WIDTH + 3 * FOX_WIDTH
F_PAD = 128
RMS_EPS = 1e-6
NEG_INF = -1e30
ATTN_SCALE = HEAD_DIM ** -0.5
ADAM_LR, ADAM_B1, ADAM_B2, ADAM_EPS, ADAM_WD, ADAM_STEP = 0.001, 0.9, 0.999, 1e-08, 0.01, 10

VMEM_LIMIT = 48 * 1024 * 1024
VMEM_LIMIT_RESIDENT = 56 * 1024 * 1024
LANES = 128
MESH = pl.DeviceIdType.MESH
HBM_SPEC = pl.BlockSpec(memory_space=pltpu.HBM)


def _pcall(body, after=None, **kw):
    if after is None:
        return pl.pallas_call(body, **kw)
    n_in = len(kw["in_specs"])
    kw["in_specs"] = list(kw["in_specs"]) + [pl.BlockSpec(memory_space=pl.ANY)]

    def tied(*refs):
        return body(*refs[:n_in], *refs[n_in + 1:])

    call = pl.pallas_call(tied, **kw)
    return lambda *args: call(*args, after)


def _params(*sem):
    return pltpu.CompilerParams(dimension_semantics=sem, vmem_limit_bytes=VMEM_LIMIT)


def _pick(dim, pref):
    t = (min(pref, dim) // 128) * 128
    while t >= 128:
        if dim % t == 0:
            return t
        t -= 128
    return dim


def _mm(a, b, *, name, ta=False, tb=False, out_dtype=F32, add=None, tm=1024, tn=512, tk=2048, after=None,
        b_blocks=False, out_blocks=None, a_halves=False, b_halves=False, rms_bwd=None, more=()):
    if a_halves:
        M, K = a.shape[1], 2 * a.shape[2]
    elif ta:
        K, M = a.shape
    else:
        M, K = a.shape
    if b_halves:
        b_rows, b_cols = b.shape[1], 2 * b.shape[2]
    else:
        b_rows, b_cols = (b.shape[1], b.shape[0] * b.shape[2]) if b_blocks else b.shape
    if tb:
        N, K2 = b_rows, b_cols
    else:
        K2, N = b_rows, b_cols
    assert K == K2, (a.shape, b.shape)
    shard = b.shape[2] if b_blocks else None
    tm = _pick(M, tm)
    tn = _pick(shard if (b_blocks and not tb) else (out_blocks or N), tn)
    tk = _pick(shard if (b_blocks and tb) else K, tk)
    nk = K // tk
    dn = (((0 if ta else 1,), (1 if tb else 0,)), ((), ()))
    has_add = add is not None
    assert not (has_add and out_blocks)
    has_norm = rms_bwd is not None
    if has_norm:
        tn = N
        assert not out_blocks and out_dtype == F32
    assert not more or (nk == 1 and tb and not ta)

    def body(*refs):
        a_ref, b_ref = refs[0], refs[1]
        rest = list(refs[2:])
        more_refs = [(rest.pop(0), rest.pop(0)) for _ in more]
        add_ref = rest.pop(0) if has_add else None
        x_ref, g_ref, dres_ref = (rest.pop(0), rest.pop(0), rest.pop(0)) if has_norm else (None, None, None)
        o_ref = rest.pop(0)
        dg_ref = rest.pop(0) if has_norm else None
        bv = b_ref[0] if b_blocks else b_ref[...]
        p = lax.dot_general(a_ref[...].astype(_CD), bv.astype(_CD), dn, preferred_element_type=F32)
        for a2_ref, b2_ref in more_refs:
            p += lax.dot_general(a2_ref[...].astype(_CD), b2_ref[...].astype(_CD), dn, preferred_element_type=F32)

        def finish(r):
            if has_add:
                r = r + add_ref[...]
            if has_norm:
                xv = x_ref[...]
                rs = lax.rsqrt(jnp.mean(xv * xv, axis=-1, keepdims=True) + RMS_EPS)
                xh = xv * rs
                dxh = r * g_ref[...]
                o_ref[...] = dres_ref[...] + rs * (dxh - xh * jnp.mean(dxh * xh, axis=-1, keepdims=True))
                part = jnp.sum(r * xh, axis=0, keepdims=True)
                first = pl.program_id(0) == 0

                @pl.when(first)
                def _():
                    dg_ref[...] = part

                @pl.when(jnp.logical_not(first))
                def _():
                    dg_ref[...] += part
            elif out_blocks:
                o_ref[0] = r.astype(out_dtype)
            else:
                o_ref[...] = r.astype(out_dtype)

        if nk == 1:
            finish(p)
        else:
            acc_ref = rest.pop(0)
            k = pl.program_id(2)

            @pl.when(k == 0)
            def _():
                acc_ref[...] = p

            @pl.when(k > 0)
            def _():
                acc_ref[...] += p

            @pl.when(k == nk - 1)
            def _():
                finish(acc_ref[...])

    if a_halves:
        ka = (K // 2) // tk
        a_spec = pl.BlockSpec((None, tm, tk), lambda i, j, k: (k // ka, i, k % ka))
    else:
        a_spec = pl.BlockSpec((tk, tm), lambda i, j, k: (k, i)) if ta else pl.BlockSpec((tm, tk), lambda i, j, k: (i, k))
    if b_halves:
        nb_ = (N // 2) // tn
        b_spec = pl.BlockSpec((None, tk, tn), lambda i, j, k: (j // nb_, k, j % nb_))
    elif b_blocks and tb:
        per = shard // tk
        b_spec = pl.BlockSpec((1, tn, tk), lambda i, j, k: (k // per, j, k % per))
    elif b_blocks:
        per = shard // tn
        b_spec = pl.BlockSpec((1, tk, tn), lambda i, j, k: (j // per, k, j % per))
    else:
        b_spec = pl.BlockSpec((tn, tk), lambda i, j, k: (j, k)) if tb else pl.BlockSpec((tk, tn), lambda i, j, k: (k, j))
    if out_blocks:
        oper = out_blocks // tn
        o_spec = pl.BlockSpec((1, tm, tn), lambda i, j, k: (j // oper, i, j % oper))
        out_shape = jax.ShapeDtypeStruct((N // out_blocks, M, out_blocks), out_dtype)
    else:
        o_spec = pl.BlockSpec((tm, tn), lambda i, j, k: (i, j))
        out_shape = jax.ShapeDtypeStruct((M, N), out_dtype)
    in_specs, args = [a_spec, b_spec], (a, b)
    for a2, b2 in more:
        assert a2.shape[0] == M and b2.shape == (N, a2.shape[1]), (a2.shape, b2.shape)
        in_specs += [pl.BlockSpec((tm, a2.shape[1]), lambda i, j, k: (i, 0)),
                     pl.BlockSpec((tn, a2.shape[1]), lambda i, j, k: (j, 0))]
        args += (a2, b2)
    if has_add:
        in_specs, args = in_specs + [o_spec], args + (add,)
    out_specs, semantics = o_spec, ("parallel", "parallel", "arbitrary")
    if has_norm:
        vec = pl.BlockSpec((1, N), lambda i, j, k: (0, 0))
        in_specs += [o_spec, vec, o_spec]
        args += tuple(rms_bwd)
        out_specs, out_shape = [o_spec, vec], [out_shape, jax.ShapeDtypeStruct((1, N), F32)]
        semantics = ("arbitrary", "arbitrary", "arbitrary")
    return _pcall(
        body, after, name=name, grid=(M // tm, N // tn, nk), in_specs=in_specs, out_specs=out_specs,
        out_shape=out_shape,
        scratch_shapes=[pltpu.VMEM((tm, tn), F32)] if nk > 1 else [],
        compiler_params=_params(*semantics),
    )(*args)


def _rms_fwd(x, g, *, name, tm=512, after=None):
    S, D = x.shape

    def body(x_ref, g_ref, h_ref):
        xv = x_ref[...]
        r = lax.rsqrt(jnp.mean(xv * xv, axis=-1, keepdims=True) + RMS_EPS)
        h_ref[...] = ((xv * r) * g_ref[...]).astype(h_ref.dtype)

    row = pl.BlockSpec((tm, D), lambda i: (i, 0))
    return _pcall(body, after, name=name, grid=(S // tm,), in_specs=[row, pl.BlockSpec((1, D), lambda i: (0, 0))],
                  out_specs=row, out_shape=jax.ShapeDtypeStruct((S, D), _CD), compiler_params=_params("parallel"))(x, g)


def _ffn_down_loss(act, w_down, x1, g, tgt, *, name, tm=512):
    S, D = x1.shape
    F = act.shape[1]

    def body(a_ref, b_ref, x_ref, g_ref, t_ref, loss_ref, dx_ref, dg_ref):
        xv = x_ref[...] + jnp.dot(a_ref[...].astype(_CD), b_ref[...].astype(_CD), preferred_element_type=F32)
        gv = g_ref[...]
        r = lax.rsqrt(jnp.mean(xv * xv, axis=-1, keepdims=True) + RMS_EPS)
        xh = xv * r
        err = xh * gv - t_ref[...]
        lpart = 0.5 * jnp.sum(jnp.mean(err * err, axis=-1, keepdims=True), axis=0, keepdims=True)
        dy = err * (1.0 / D)
        dxh = dy * gv
        dx_ref[...] = r * (dxh - xh * jnp.mean(dxh * xh, axis=-1, keepdims=True))
        gpart = jnp.sum(dy * xh, axis=0, keepdims=True)

        @pl.when(pl.program_id(0) == 0)
        def _():
            loss_ref[...] = lpart
            dg_ref[...] = gpart

        @pl.when(pl.program_id(0) > 0)
        def _():
            loss_ref[...] += lpart
            dg_ref[...] += gpart

    row = pl.BlockSpec((tm, D), lambda i: (i, 0))
    vec = pl.BlockSpec((1, D), lambda i: (0, 0))
    one = pl.BlockSpec((1, 1), lambda i: (0, 0))
    return _pcall(body, name=name, grid=(S // tm,),
                  in_specs=[pl.BlockSpec((tm, F), lambda i: (i, 0)), pl.BlockSpec((F, D), lambda i: (0, 0)), row, vec, row],
                  out_specs=[one, row, vec],
                  out_shape=[jax.ShapeDtypeStruct((1, 1), F32), jax.ShapeDtypeStruct((S, D), F32),
                             jax.ShapeDtypeStruct((1, D), F32)],
                  compiler_params=_params("arbitrary"))(act, w_down, x1, g, tgt)


def _sigmoid(z):
    return 1.0 / (1.0 + jnp.exp(-z))


def _gated_mix_out(gl, bg, ya, yb, w_out, x, g, *, name, tm=512):
    S, D = ya.shape

    def body(za_ref, zb_ref, ba_ref, bb_ref, ya_ref, yb_ref, w_ref, x_ref, g_ref, m_ref, x1_ref, h_ref):
        ga = _sigmoid(za_ref[...].astype(F32) + ba_ref[...])
        gb = _sigmoid(zb_ref[...].astype(F32) + bb_ref[...])
        merged = (ga * ya_ref[...].astype(F32) + gb * yb_ref[...].astype(F32)).astype(m_ref.dtype)
        m_ref[...] = merged
        x1 = x_ref[...] + jnp.dot(merged, w_ref[...].astype(_CD), preferred_element_type=F32)
        x1_ref[...] = x1
        rs = lax.rsqrt(jnp.mean(x1 * x1, axis=-1, keepdims=True) + RMS_EPS)
        h_ref[...] = ((x1 * rs) * g_ref[...]).astype(h_ref.dtype)

    lo = pl.BlockSpec((tm, D), lambda i: (i, 0))
    hi = pl.BlockSpec((tm, D), lambda i: (i, 1))
    vlo = pl.BlockSpec((1, D), lambda i: (0, 0))
    vhi = pl.BlockSpec((1, D), lambda i: (0, 1))
    whole = pl.BlockSpec((D, D), lambda i: (0, 0))
    return _pcall(body, name=name, grid=(S // tm,), in_specs=[lo, hi, vlo, vhi, lo, lo, whole, lo, vlo],
                  out_specs=[lo, lo, lo],
                  out_shape=[jax.ShapeDtypeStruct((S, D), _CD), jax.ShapeDtypeStruct((S, D), F32),
                             jax.ShapeDtypeStruct((S, D), _CD)],
                  compiler_params=_params("parallel"))(gl, gl, bg, bg, ya, yb, w_out, x, g)


def _gate_bwd(dx1, w_out, gl, bg, ya, yb, *, name, tm=512):
    S, D = ya.shape
    nt = (((1,), (1,)), ((), ()))

    def body(dx_ref, w_ref, za_ref, zb_ref, ba_ref, bb_ref, ya_ref, yb_ref, dya_ref, dyb_ref, dgl_ref, dbg_ref):
        dmv = lax.dot_general(dx_ref[...].astype(_CD), w_ref[...].astype(_CD), nt, preferred_element_type=F32)
        ga = _sigmoid(za_ref[...].astype(F32) + ba_ref[...])
        gb = _sigmoid(zb_ref[...].astype(F32) + bb_ref[...])
        dya_ref[...] = (dmv * ga).astype(dya_ref.dtype)
        dyb_ref[...] = (dmv * gb).astype(dyb_ref.dtype)
        dza = dmv * ya_ref[...].astype(F32) * ga * (1.0 - ga)
        dzb = dmv * yb_ref[...].astype(F32) * gb * (1.0 - gb)
        dgl_ref[:, :D] = dza.astype(dgl_ref.dtype)
        dgl_ref[:, D:] = dzb.astype(dgl_ref.dtype)
        pa = jnp.sum(dza, axis=0, keepdims=True)
        pb = jnp.sum(dzb, axis=0, keepdims=True)

        @pl.when(pl.program_id(0) == 0)
        def _():
            dbg_ref[:, :D] = pa
            dbg_ref[:, D:] = pb

        @pl.when(pl.program_id(0) > 0)
        def _():
            dbg_ref[:, :D] += pa
            dbg_ref[:, D:] += pb

    lo = pl.BlockSpec((tm, D), lambda i: (i, 0))
    hi = pl.BlockSpec((tm, D), lambda i: (i, 1))
    vlo = pl.BlockSpec((1, D), lambda i: (0, 0))
    vhi = pl.BlockSpec((1, D), lambda i: (0, 1))
    wide = pl.BlockSpec((tm, 2 * D), lambda i: (i, 0))
    vwide = pl.BlockSpec((1, 2 * D), lambda i: (0, 0))
    whole = pl.BlockSpec((D, D), lambda i: (0, 0))
    return _pcall(body, name=name, grid=(S // tm,), in_specs=[lo, whole, lo, hi, vlo, vhi, lo, lo],
                  out_specs=[lo, lo, wide, vwide],
                  out_shape=[jax.ShapeDtypeStruct((S, D), _CD), jax.ShapeDtypeStruct((S, D), _CD),
                             jax.ShapeDtypeStruct((S, 2 * D), _CD), jax.ShapeDtypeStruct((1, 2 * D), F32)],
                  compiler_params=_params("arbitrary"))(dx1, w_out, gl, gl, bg, bg, ya, yb)


def _ffn_in_act(h2, w_blocks, *, name, tm=512):
    S, D = h2.shape
    _, _, C = w_blocks.shape

    def body(a_ref, bg_ref, bu_ref, g_ref, u_ref, o_ref):
        av = a_ref[...].astype(_CD)
        gv = jnp.dot(av, bg_ref[0].astype(_CD), preferred_element_type=F32)
        uv = jnp.dot(av, bu_ref[0].astype(_CD), preferred_element_type=F32)
        g_ref[...] = gv.astype(g_ref.dtype)
        u_ref[...] = uv.astype(u_ref.dtype)
        o_ref[...] = (gv * _sigmoid(gv) * uv).astype(o_ref.dtype)

    out = pl.BlockSpec((tm, C), lambda i, j: (i, j))
    shp = jax.ShapeDtypeStruct((S, 2 * C), _CD)
    return _pcall(body, name=name, grid=(S // tm, 2),
                  in_specs=[pl.BlockSpec((tm, D), lambda i, j: (i, 0)), pl.BlockSpec((1, D, C), lambda i, j: (j, 0, 0)),
                            pl.BlockSpec((1, D, C), lambda i, j: (2 + j, 0, 0))],
                  out_specs=[out, out, out], out_shape=[shp, shp, shp],
                  compiler_params=_params("parallel", "arbitrary"))(h2, w_blocks, w_blocks)


def _d_swiglu(dx, w_down, gate, up, *, name, tm=512, tn=1408):
    S, D = dx.shape
    F = w_down.shape[0]
    nt = (((1,), (1,)), ((), ()))

    def body(a_ref, b_ref, g_ref, u_ref, o_ref):
        dv = lax.dot_general(a_ref[...].astype(_CD), b_ref[...].astype(_CD), nt, preferred_element_type=F32)
        gv = g_ref[...].astype(F32)
        sg = _sigmoid(gv)
        o_ref[0] = (dv * u_ref[...].astype(F32) * (sg * (1.0 + gv * (1.0 - sg)))).astype(o_ref.dtype)
        o_ref[1] = (dv * (gv * sg)).astype(o_ref.dtype)

    tile = pl.BlockSpec((tm, tn), lambda i, j: (i, j))
    return _pcall(body, name=name, grid=(S // tm, F // tn),
                  in_specs=[pl.BlockSpec((tm, D), lambda i, j: (i, 0)), pl.BlockSpec((tn, D), lambda i, j: (j, 0)),
                            tile, tile],
                  out_specs=pl.BlockSpec((2, tm, tn), lambda i, j: (0, i, j)),
                  out_shape=jax.ShapeDtypeStruct((2, S, F), _CD),
                  compiler_params=_params("parallel", "arbitrary"))(dx, w_down, gate, up)


def _split3(x):
    hi = x.astype(jnp.bfloat16)
    r1 = x - hi.astype(F32)
    mid = r1.astype(jnp.bfloat16)
    lo = (r1 - mid.astype(F32)).astype(jnp.bfloat16)
    return hi, mid, lo


def _ones_dot_left(ones, x):
    return sum(jnp.dot(ones, p, preferred_element_type=F32) for p in _split3(x))


def _ones_dot_right(x, ones):
    return sum(jnp.dot(p, ones, preferred_element_type=F32) for p in _split3(x))


def _head_sum(x):
    n = x.shape[1]
    r = lax.broadcasted_iota(jnp.int32, (n, n), 0) // HEAD_DIM
    c = lax.broadcasted_iota(jnp.int32, (n, n), 1) // HEAD_DIM
    return _ones_dot_right(x, (r == c).astype(jnp.bfloat16))


def _log_sigmoid(z):
    e = jnp.exp(-jnp.abs(z))
    t = 1.0 + e
    log1p_e = jnp.where(t == 1.0, e, jnp.log(t) * (e / jnp.where(t == 1.0, 1.0, t - 1.0)))
    return jnp.minimum(z, 0.0) - log1p_e


def _fox_cumsum(zf, bf, *, name):
    S, W = zf.shape
    nb = S // 128

    def body(z_ref, b_ref, c_ref):
        tri = (lax.broadcasted_iota(jnp.int32, (128, 128), 0) >= lax.broadcasted_iota(jnp.int32, (128, 128), 1))
        tri = tri.astype(jnp.bfloat16)

        def step(i, carry):
            rows = pl.ds(pl.multiple_of(i * 128, 128), 128)
            lf = _log_sigmoid(z_ref[rows, :] + b_ref[...])
            cb = _ones_dot_left(tri, lf) + carry
            c_ref[rows, :] = cb
            return cb[127:128, :]

        lax.fori_loop(0, nb, step, jnp.zeros((1, W), F32))

    return _pcall(body, name=name, out_shape=jax.ShapeDtypeStruct((S, W), F32),
                  compiler_params=pltpu.CompilerParams(vmem_limit_bytes=VMEM_LIMIT))(zf, bf)


def _fox_cumsum_bwd(dc, zf, bf, *, name):
    S, W = zf.shape
    nb = S // 128

    def body(dc_ref, z_ref, b_ref, dz_ref, db_ref):
        tri = (lax.broadcasted_iota(jnp.int32, (128, 128), 0) <= lax.broadcasted_iota(jnp.int32, (128, 128), 1))
        tri = tri.astype(jnp.bfloat16)

        def step(k, carry):
            tail, acc = carry
            i = nb - 1 - k
            rows = pl.ds(pl.multiple_of(i * 128, 128), 128)
            dlf = _ones_dot_left(tri, dc_ref[rows, :]) + tail
            dz = dlf * _sigmoid(-(z_ref[rows, :] + b_ref[...]))
            dz_ref[rows, :] = dz
            return dlf[0:1, :], acc + jnp.sum(dz, axis=0, keepdims=True)

        _, acc = lax.fori_loop(0, nb, step, (jnp.zeros((1, W), F32), jnp.zeros((1, W), F32)))
        db_ref[...] = acc

    return _pcall(body, name=name,
                  out_shape=[jax.ShapeDtypeStruct((S, W), F32), jax.ShapeDtypeStruct((1, W), F32)],
                  compiler_params=pltpu.CompilerParams(vmem_limit_bytes=VMEM_LIMIT))(dc, zf, bf)


def _proj_dil(h, w_qkv, *, name, tm=1024):
    S, D = h.shape
    tn = DIL_WIDTH

    def body(a_ref, b_ref, *rest):
        outs, acc = rest[:N_DIL_GROUPS], rest[N_DIL_GROUPS]
        prod = jnp.dot(a_ref[...].astype(_CD), b_ref[...].astype(_CD), preferred_element_type=F32)
        for k in range(tn // LANES):
            acc[k] = prod[:, k * LANES:(k + 1) * LANES]
        for g, (_, d) in enumerate(DIL_PAIRS):
            for half in range(DIL_OUT // LANES):
                k = g * (DIL_OUT // LANES) + half
                cols = slice(half * LANES, (half + 1) * LANES)
                for r in range(d):
                    rows = pl.ds(r, tm // d, stride=d) if d > 1 else slice(None)
                    outs[g][0, r, :, cols] = acc[k, rows, :].astype(outs[g].dtype)

    out_specs = [pl.BlockSpec((1, d, tm // d, DIL_OUT), lambda i, j: (j, 0, i, 0)) for _, d in DIL_PAIRS]
    out_shape = [jax.ShapeDtypeStruct((3, d, S // d, DIL_OUT), _CD) for _, d in DIL_PAIRS]
    outs = _pcall(body, name=name, grid=(S // tm, 3),
                  in_specs=[pl.BlockSpec((tm, D), lambda i, j: (i, 0)), pl.BlockSpec((D, tn), lambda i, j: (0, j))],
                  out_specs=out_specs, out_shape=out_shape, scratch_shapes=[pltpu.VMEM((tn // LANES, tm, LANES), F32)],
                  compiler_params=_params("parallel", "arbitrary"))(h, w_qkv)
    return [o.reshape(3, S, DIL_OUT) for o in outs]


def _dil_start(block, S, dilation):
    sub = S // dilation
    u0 = block * DIL_W
    return (u0 % sub) * dilation + u0 // sub


def _dil_slopes(group):
    h = np.arange(1, N_DIL_GROUPS * DIL_HEADS + 1, dtype=np.float32)
    s = (np.float32(2.0) ** (np.float32(-8.0) * h / np.float32(N_DIL_GROUPS * DIL_HEADS))).astype(np.float32)
    return [float(v) for v in s.reshape(N_DIL_GROUPS, DIL_HEADS)[group]]


def _dil_tiles(i, n, blocks_per_seq):
    qi = lax.broadcasted_iota(jnp.int32, (DIL_W, 2 * DIL_W), 0)
    kj = lax.broadcasted_iota(jnp.int32, (DIL_W, 2 * DIL_W), 1)
    rel = qi + DIL_W - kj
    first = ((4 * n + i) % blocks_per_seq) == 0
    valid = jnp.logical_and(jnp.logical_and(rel >= 0, rel <= DIL_W), jnp.logical_or(kj >= DIL_W, jnp.logical_not(first)))
    return valid, rel.astype(F32)


def _dil_window(cur_ref, prev_ref, i, cols):
    if i > 0:
        return cur_ref[(i - 1) * DIL_W:(i + 1) * DIL_W, cols]
    return jnp.concatenate([prev_ref[:, cols], cur_ref[:DIL_W, cols]], axis=0)


CHUNK = 4 * DIL_W


def _dil_rows(block, S, dilation):
    start = _dil_start(block, S, dilation)
    return pl.ds(start, DIL_W, stride=dilation) if dilation > 1 else pl.ds(start, DIL_W)


def SPLIT(S):
    return (DIL_OUT // LANES, S, LANES)


def _dil_fwd(qkv, group, *, name):
    S = qkv.shape[1]
    dilation = DIL_PAIRS[group][1]
    bps = (S // dilation) // DIL_W
    slopes = _dil_slopes(group)
    nt = (((1,), (1,)), ((), ()))

    def body(q_ref, k_ref, v_ref, kp_ref, vp_ref, on_ref, ln_ref, o_ref, l_ref):
        n = pl.program_id(0)
        for i in range(4):
            valid, rel = _dil_tiles(i, n, bps)
            rows = slice(i * DIL_W, (i + 1) * DIL_W)
            for h in range(DIL_HEADS):
                cols = slice(h * HEAD_DIM, (h + 1) * HEAD_DIM)
                qh = q_ref[rows, cols]
                k2, v2 = _dil_window(k_ref, kp_ref, i, cols), _dil_window(v_ref, vp_ref, i, cols)
                s = lax.dot_general(qh, k2, nt, preferred_element_type=F32) * ATTN_SCALE - (slopes[h] * dilation) * rel
                s = jnp.where(valid, s, NEG_INF)
                m = jnp.max(s, axis=-1, keepdims=True)
                p = jnp.exp(s - m)
                den = jnp.sum(p, axis=-1, keepdims=True)
                acc = jnp.dot(p.astype(_CD), v2, preferred_element_type=F32)
                o_ref[rows, cols] = acc / den
                l_ref[rows, cols] = jnp.broadcast_to(m + jnp.log(den), (DIL_W, HEAD_DIM))
        for i in range(4):
            rows = slice(i * DIL_W, (i + 1) * DIL_W)
            nat = _dil_rows(4 * n + i, S, dilation)
            for half in range(DIL_OUT // LANES):
                cols = slice(half * LANES, (half + 1) * LANES)
                on_ref[half, nat, :] = o_ref[rows, cols]
                ln_ref[half, nat, :] = l_ref[rows, cols]

    def cur(which):
        return pl.BlockSpec((None, CHUNK, DIL_OUT), lambda n: (which, n, 0))

    def prev(which):
        return pl.BlockSpec((None, DIL_W, DIL_OUT), lambda n: (which, jnp.maximum(4 * n - 1, 0), 0))

    whole = pl.BlockSpec(SPLIT(S), lambda n: (0, 0, 0))
    return _pcall(body, name=name, grid=(S // CHUNK,), in_specs=[cur(0), cur(1), cur(2), prev(1), prev(2)],
                  out_specs=[whole, whole],
                  out_shape=[jax.ShapeDtypeStruct(SPLIT(S), F32), jax.ShapeDtypeStruct(SPLIT(S), F32)],
                  scratch_shapes=[pltpu.VMEM((CHUNK, DIL_OUT), F32), pltpu.VMEM((CHUNK, DIL_OUT), F32)],
                  compiler_params=_params("arbitrary"))(qkv, qkv, qkv, qkv, qkv)


STAT_OFFSET = HEAD_DIM // 2


def _dil_bwd(qkv, stats, do, group, *, name):
    S = qkv.shape[1]
    dilation = DIL_PAIRS[group][1]
    bps = (S // dilation) // DIL_W
    slopes = _dil_slopes(group)
    nchunk = S // CHUNK
    nt = (((1,), (1,)), ((), ()))
    tn = (((0,), (0,)), ((), ()))

    def body(q_ref, k_ref, v_ref, kp_ref, vp_ref, ln_ref, don_ref, dqn_ref, dkn_ref, dvn_ref,
             dk_s, dv_s, l_ref, do_ref, dq_ref):
        step = pl.program_id(0)
        n = nchunk - 1 - step
        for i in range(4):
            rows = slice(i * DIL_W, (i + 1) * DIL_W)
            nat = _dil_rows(4 * n + i, S, dilation)
            for half in range(DIL_OUT // LANES):
                cols = slice(half * LANES, (half + 1) * LANES)
                l_ref[rows, cols] = ln_ref[half, nat, :]
                do_ref[rows, cols] = don_ref[half, nat, :]

        @pl.when(step == 0)
        def _():
            dk_s[:, CHUNK:] = jnp.zeros((DIL_OUT, DIL_W), F32)
            dv_s[:, CHUNK:] = jnp.zeros((DIL_OUT, DIL_W), F32)

        dk_s[:, :CHUNK] = jnp.zeros((DIL_OUT, CHUNK), F32)
        dv_s[:, :CHUNK] = jnp.zeros((DIL_OUT, CHUNK), F32)
        for i in range(4):
            valid, rel = _dil_tiles(i, n, bps)
            rows = slice(i * DIL_W, (i + 1) * DIL_W)
            window = slice(i * DIL_W, (i + 2) * DIL_W)
            for h in range(DIL_HEADS):
                cols = slice(h * HEAD_DIM, (h + 1) * HEAD_DIM)
                qh = q_ref[rows, cols]
                k2, v2 = _dil_window(k_ref, kp_ref, i, cols), _dil_window(v_ref, vp_ref, i, cols)
                lh = l_ref[rows, h * HEAD_DIM:h * HEAD_DIM + 1]
                shift = l_ref[rows, h * HEAD_DIM + STAT_OFFSET:h * HEAD_DIM + STAT_OFFSET + 1]
                s = lax.dot_general(qh, k2, nt, preferred_element_type=F32) * ATTN_SCALE - (slopes[h] * dilation) * rel
                p = jnp.exp(jnp.where(valid, s, NEG_INF) - lh)
                dob = do_ref[rows, cols].astype(_CD)
                ds = p * (lax.dot_general(dob, v2, nt, preferred_element_type=F32) + shift)
                dsb = (ds * ATTN_SCALE).astype(_CD)
                dq_ref[rows, cols] = jnp.dot(dsb, k2, preferred_element_type=F32)
                dk_s[cols, window] += lax.dot_general(qh, dsb, tn, preferred_element_type=F32)
                dv_s[cols, window] += lax.dot_general(dob, p.astype(_CD), tn, preferred_element_type=F32)
        for i in range(4):
            rows = slice(i * DIL_W, (i + 1) * DIL_W)
            done = slice((i + 1) * DIL_W, (i + 2) * DIL_W)
            nat = _dil_rows(4 * n + i, S, dilation)
            dkb, dvb = dk_s[:, done].T, dv_s[:, done].T
            for half in range(DIL_OUT // LANES):
                cols = slice(half * LANES, (half + 1) * LANES)
                dqn_ref[half, nat, :] = dq_ref[rows, cols]
                dkn_ref[half, nat, :] = dkb[:, cols]
                dvn_ref[half, nat, :] = dvb[:, cols]
        dk_s[:, CHUNK:] = dk_s[:, :DIL_W]
        dv_s[:, CHUNK:] = dv_s[:, :DIL_W]

    def cur(which):
        return pl.BlockSpec((None, CHUNK, DIL_OUT), lambda s: (which, nchunk - 1 - s, 0))

    def prev(which):
        return pl.BlockSpec((None, DIL_W, DIL_OUT), lambda s: (which, jnp.maximum(4 * (nchunk - 1 - s) - 1, 0), 0))

    whole = pl.BlockSpec(SPLIT(S), lambda s: (0, 0, 0))
    shp = jax.ShapeDtypeStruct(SPLIT(S), F32)
    tile = pltpu.VMEM((CHUNK, DIL_OUT), F32)
    return _pcall(body, name=name, grid=(nchunk,),
                  in_specs=[cur(0), cur(1), cur(2), prev(1), prev(2), whole, whole],
                  out_specs=[whole, whole, whole], out_shape=[shp, shp, shp],
                  scratch_shapes=[pltpu.VMEM((DIL_OUT, CHUNK + DIL_W), F32), pltpu.VMEM((DIL_OUT, CHUNK + DIL_W), F32),
                                  tile, tile, tile],
                  compiler_params=pltpu.CompilerParams(dimension_semantics=("arbitrary",),
                                                       vmem_limit_bytes=VMEM_LIMIT_RESIDENT))(
        qkv, qkv, qkv, qkv, qkv, stats, do)


def _dil_mix_fwd(os_, ls_, *, name, tm=512):
    nh, S, _ = os_[0].shape

    def body(o0, o1, o2, l0, l1, l2, out_ref):
        for half in range(nh):
            ls = [l0[half], l1[half], l2[half]]
            m = jnp.maximum(jnp.maximum(ls[0], ls[1]), ls[2])
            es = [jnp.exp(l - m) for l in ls]
            den = es[0] + es[1] + es[2]
            mixed = (es[0] * o0[half] + es[1] * o1[half] + es[2] * o2[half]) / den
            out_ref[:, half * LANES:(half + 1) * LANES] = mixed.astype(out_ref.dtype)

    halves = pl.BlockSpec((nh, tm, LANES), lambda i: (0, i, 0))
    row = pl.BlockSpec((tm, nh * LANES), lambda i: (i, 0))
    return _pcall(body, name=name, grid=(S // tm,), in_specs=[halves] * 6, out_specs=row,
                  out_shape=jax.ShapeDtypeStruct((S, nh * LANES), _CD), compiler_params=_params("parallel"))(*os_, *ls_)


def _dil_mix_bwd(doa, os_, ls_, *, name, tm=512, after=None):
    nh, S, _ = os_[0].shape

    def body(d_ref, o0, o1, o2, l0, l1, l2, do0, do1, do2, st0, st1, st2):
        first = lax.broadcasted_iota(jnp.int32, (tm, LANES), 1) % HEAD_DIM < STAT_OFFSET
        for half in range(nh):
            dv = d_ref[:, half * LANES:(half + 1) * LANES]
            ls = [l0[half], l1[half], l2[half]]
            m = jnp.maximum(jnp.maximum(ls[0], ls[1]), ls[2])
            es = [jnp.exp(l - m) for l in ls]
            den = es[0] + es[1] + es[2]
            al = [e / den for e in es]
            da = [_head_sum(dv * o[half]) for o in (o0, o1, o2)]
            mean = al[0] * da[0] + al[1] * da[1] + al[2] * da[2]
            for a, l, do_ref, st_ref in zip(al, ls, (do0, do1, do2), (st0, st1, st2)):
                do_ref[half] = a * dv
                st_ref[half] = jnp.where(first, l, -a * mean)

    halves = pl.BlockSpec((nh, tm, LANES), lambda i: (0, i, 0))
    row = pl.BlockSpec((tm, nh * LANES), lambda i: (i, 0))
    shp = jax.ShapeDtypeStruct((nh, S, LANES), F32)
    return _pcall(body, after, name=name, grid=(S // tm,), in_specs=[row] + [halves] * 6, out_specs=[halves] * 6,
                  out_shape=[shp] * 6, compiler_params=_params("parallel"))(doa, *os_, *ls_)


FOX_T = 512


PACK = 2 * HEAD_DIM
HEAD_PAIRS = N_FOX_HEADS // 2
FOX_HPS = 8
Q_BLOCK0 = 0
K_BLOCK0 = FOX_WIDTH // PACK
V_BLOCK0 = 2 * FOX_WIDTH // PACK


def _pieces(x):
    hi = x.astype(jnp.bfloat16).astype(F32)
    r = x - hi
    mid = r.astype(jnp.bfloat16).astype(F32)
    lo = (r - mid).astype(jnp.bfloat16).astype(F32)
    return [hi, mid, lo]


def _extras(first, second, rows):
    lane = lax.broadcasted_iota(jnp.int32, (rows, HEAD_DIM), 1)
    out = jnp.zeros((rows, HEAD_DIM), F32)
    for base, triple in ((0, first), (3, second)):
        if all(isinstance(v, float) for v in triple) and len(set(triple)) == 1:
            if triple[0] != 0.0:
                out = jnp.where(jnp.logical_and(lane >= base, lane < base + 3), triple[0], out)
        else:
            for idx, val in enumerate(triple):
                out = jnp.where(lane == base + idx, val, out)
    return out


def _head_column(c, h):
    lane = lax.broadcasted_iota(jnp.int32, c.shape, 1)
    return jnp.sum(jnp.where(lane == h, c, 0.0), axis=1, keepdims=True)


ONES3 = [1.0, 1.0, 1.0]
ZEROS3 = [0.0, 0.0, 0.0]


def _fox_pack_fwd(qkv, c, *, name, tm=1024):
    S = qkv.shape[0]

    def body(q_ref, k_ref, v_ref, c_ref, qo_ref, ko_ref, vo_ref):
        hp = pl.program_id(1)
        cv = c_ref[...]
        v_extras = jnp.where(lax.broadcasted_iota(jnp.int32, (tm, HEAD_DIM), 1) < 3, 1.0, 0.0).astype(vo_ref.dtype)
        for hh in range(2):
            ch = _pieces(_head_column(cv, 2 * hp + hh))
            src = slice(hh * HEAD_DIM, (hh + 1) * HEAD_DIM)
            lo = slice(hh * PACK, hh * PACK + HEAD_DIM)
            hi = slice(hh * PACK + HEAD_DIM, (hh + 1) * PACK)
            qo_ref[:, lo] = (q_ref[:, src].astype(F32) * ATTN_SCALE).astype(qo_ref.dtype)
            qo_ref[:, hi] = _extras(ch, ONES3, tm).astype(qo_ref.dtype)
            ko_ref[:, lo] = k_ref[:, src]
            ko_ref[:, hi] = _extras(ONES3, [-p for p in ch], tm).astype(ko_ref.dtype)
            vo_ref[:, lo] = v_ref[:, src]
            vo_ref[:, hi] = v_extras

    def src(block0):
        return pl.BlockSpec((tm, PACK), lambda i, hp: (i, block0 + hp))

    out = pl.BlockSpec((tm, 2 * PACK), lambda i, hp: (i, hp))
    shp = jax.ShapeDtypeStruct((S, N_FOX_HEADS * PACK), _CD)
    return _pcall(body, name=name, grid=(S // tm, HEAD_PAIRS),
                  in_specs=[src(Q_BLOCK0), src(K_BLOCK0), src(V_BLOCK0), pl.BlockSpec((tm, PACK), lambda i, hp: (i, 0))],
                  out_specs=[out, out, out], out_shape=[shp, shp, shp],
                  compiler_params=_params("parallel", "parallel"))(qkv, qkv, qkv, c)


def _fox_fwd(qp, kp, vp, *, name):
    S = qp.shape[0]
    nt = S // FOX_T
    nt_dims = (((1,), (1,)), ((), ()))
    tn_dims = (((0,), (0,)), ((), ()))

    def body(i_tab, j_tab, q_ref, k_ref, v_ref, o_ref, l_ref, m_s, acc_s):
        t = pl.program_id(1)
        i, j = i_tab[t], j_tab[t]

        @pl.when(j == 0)
        def _():
            m_s[...] = jnp.full((FOX_HPS, 1, FOX_T), NEG_INF, F32)
            acc_s[...] = jnp.zeros((FOX_HPS, PACK, FOX_T), F32)

        def tile(diagonal):
            for hh in range(FOX_HPS):
                cols = slice(hh * PACK, (hh + 1) * PACK)
                st = lax.dot_general(k_ref[:, cols], q_ref[:, cols], nt_dims, preferred_element_type=F32)
                if diagonal:
                    key = lax.broadcasted_iota(jnp.int32, (FOX_T, FOX_T), 0)
                    qry = lax.broadcasted_iota(jnp.int32, (FOX_T, FOX_T), 1)
                    st = jnp.where(key <= qry, st, NEG_INF)
                m_old = m_s[hh]
                m_new = jnp.maximum(m_old, jnp.max(st, axis=0, keepdims=True))
                pt = jnp.exp(st - m_new)
                acc_s[hh] = jnp.exp(m_old - m_new) * acc_s[hh] + lax.dot_general(
                    v_ref[:, cols], pt.astype(_CD), tn_dims, preferred_element_type=F32)
                m_s[hh] = m_new

        @pl.when(j < i)
        def _():
            tile(False)

        @pl.when(j == i)
        def _():
            tile(True)
            for hh in range(FOX_HPS):
                acc = acc_s[hh]
                den = acc[HEAD_DIM:HEAD_DIM + 1, :]
                cols = slice(hh * HEAD_DIM, (hh + 1) * HEAD_DIM)
                o_ref[:, cols] = (acc[:HEAD_DIM, :] / den).T
                l_ref[:, cols] = jnp.broadcast_to(m_s[hh] + jnp.log(den), (HEAD_DIM, FOX_T)).T

    pairs = [(i, j) for i in range(nt) for j in range(i + 1)]
    i_tab = jnp.asarray([p[0] for p in pairs], jnp.int32)
    j_tab = jnp.asarray([p[1] for p in pairs], jnp.int32)
    qs = pl.BlockSpec((FOX_T, FOX_HPS * PACK), lambda hp, t, it, jt: (it[t], hp))
    ks = pl.BlockSpec((FOX_T, FOX_HPS * PACK), lambda hp, t, it, jt: (jt[t], hp))
    os_ = pl.BlockSpec((FOX_T, FOX_HPS * HEAD_DIM), lambda hp, t, it, jt: (it[t], hp))
    shp = jax.ShapeDtypeStruct((S, FOX_WIDTH), F32)
    grid_spec = pltpu.PrefetchScalarGridSpec(
        num_scalar_prefetch=2, grid=(N_FOX_HEADS // FOX_HPS, len(pairs)), in_specs=[qs, ks, ks], out_specs=[os_, os_],
        scratch_shapes=[pltpu.VMEM((FOX_HPS, 1, FOX_T), F32), pltpu.VMEM((FOX_HPS, PACK, FOX_T), F32)])
    return _pcall(body, name=name, grid_spec=grid_spec, out_shape=[shp, shp],
                  compiler_params=_params("parallel", "arbitrary"))(i_tab, j_tab, qp, kp, vp)


def _fox_pack_bwd(qkv, c, o, lse, do, *, name, tm=1024, after=None):
    S = qkv.shape[0]

    def body(q_ref, c_ref, o_ref, l_ref, do_ref, qo_ref, do_out_ref):
        hp = pl.program_id(1)
        cv = c_ref[...]
        for hh in range(2):
            src = slice(hh * HEAD_DIM, (hh + 1) * HEAD_DIM)
            lo = slice(hh * PACK, hh * PACK + HEAD_DIM)
            hi = slice(hh * PACK + HEAD_DIM, (hh + 1) * PACK)
            shift = _head_column(cv, 2 * hp + hh) - l_ref[:, hh * HEAD_DIM:hh * HEAD_DIM + 1]
            dov = do_ref[:, src]
            dsum = jnp.sum(dov * o_ref[:, src], axis=-1, keepdims=True)
            qo_ref[:, lo] = (q_ref[:, src].astype(F32) * ATTN_SCALE).astype(qo_ref.dtype)
            qo_ref[:, hi] = _extras(_pieces(shift), ONES3, tm).astype(qo_ref.dtype)
            do_out_ref[:, lo] = dov.astype(do_out_ref.dtype)
            do_out_ref[:, hi] = _extras(_pieces(-dsum), ZEROS3, tm).astype(do_out_ref.dtype)

    pair = pl.BlockSpec((tm, PACK), lambda i, hp: (i, hp))
    out = pl.BlockSpec((tm, 2 * PACK), lambda i, hp: (i, hp))
    shp = jax.ShapeDtypeStruct((S, N_FOX_HEADS * PACK), _CD)
    return _pcall(body, after, name=name, grid=(S // tm, HEAD_PAIRS),
                  in_specs=[pl.BlockSpec((tm, PACK), lambda i, hp: (i, Q_BLOCK0 + hp)),
                            pl.BlockSpec((tm, PACK), lambda i, hp: (i, 0)), pair, pair, pair],
                  out_specs=[out, out], out_shape=[shp, shp],
                  compiler_params=_params("parallel", "parallel"))(qkv, c, o, lse, do)


def _fox_bwd(qp, kp, vp, dop, *, name):
    S = qp.shape[0]
    nt = S // FOX_T
    nt_dims = (((1,), (1,)), ((), ()))
    tn_dims = (((0,), (0,)), ((), ()))

    def body(i_tab, j_tab, q_ref, k_ref, v_ref, do_ref, dq_ref, dk_ref, dv_ref, dc_ref, dr_ref,
             dq_s, dk_s, dv_s, dc_s, dr_s):
        t = pl.program_id(1)
        i, j = i_tab[t], j_tab[t]

        @pl.when(t == 0)
        def _():
            dq_s[...] = jnp.zeros((S, FOX_HPS * PACK), F32)
            dr_s[...] = jnp.zeros((FOX_HPS, 1, S), F32)

        @pl.when(i == j)
        def _():
            dk_s[...] = jnp.zeros((FOX_T, FOX_HPS * PACK), F32)
            dv_s[...] = jnp.zeros((FOX_T, FOX_HPS * PACK), F32)
            dc_s[...] = jnp.zeros((FOX_HPS, FOX_T, 1), F32)

        def tile(diagonal):
            rows = pl.ds(pl.multiple_of(i * FOX_T, FOX_T), FOX_T)
            for hh in range(FOX_HPS):
                cols = slice(hh * PACK, (hh + 1) * PACK)
                qv, kv, vv, dov = q_ref[:, cols], k_ref[:, cols], v_ref[:, cols], do_ref[:, cols]
                pt = jnp.exp(lax.dot_general(kv, qv, nt_dims, preferred_element_type=F32))
                if diagonal:
                    key = lax.broadcasted_iota(jnp.int32, (FOX_T, FOX_T), 0)
                    qry = lax.broadcasted_iota(jnp.int32, (FOX_T, FOX_T), 1)
                    pt = jnp.where(key <= qry, pt, 0.0)
                dst = pt * lax.dot_general(vv, dov, nt_dims, preferred_element_type=F32)
                dsb = dst.astype(_CD)
                dc_s[hh] += jnp.sum(dst, axis=1, keepdims=True)
                dr_s[hh, :, rows] += jnp.sum(dst, axis=0, keepdims=True)
                dv_s[:, cols] += jnp.dot(pt.astype(_CD), dov, preferred_element_type=F32)
                dk_s[:, cols] += jnp.dot(dsb, qv, preferred_element_type=F32)
                dq_s[rows, cols] += lax.dot_general(dsb, kv, tn_dims, preferred_element_type=F32)

        @pl.when(i > j)
        def _():
            tile(False)

        @pl.when(i == j)
        def _():
            tile(True)

        @pl.when(i == nt - 1)
        def _():
            for hh in range(FOX_HPS):
                src = slice(hh * PACK, hh * PACK + HEAD_DIM)
                dst_cols = slice(hh * HEAD_DIM, (hh + 1) * HEAD_DIM)
                dk_ref[:, dst_cols] = dk_s[:, src].astype(dk_ref.dtype)
                dv_ref[:, dst_cols] = dv_s[:, src].astype(dv_ref.dtype)
                dc_ref[:, dst_cols] = jnp.broadcast_to(dc_s[hh], (FOX_T, HEAD_DIM))

        @pl.when(t == len(pairs) - 1)
        def _():
            for hh in range(FOX_HPS):
                dq_ref[:, hh * HEAD_DIM:(hh + 1) * HEAD_DIM] = (
                    dq_s[:, hh * PACK:hh * PACK + HEAD_DIM] * ATTN_SCALE).astype(dq_ref.dtype)
            dr_ref[...] = dr_s[...]

    pairs = [(i, j) for j in range(nt) for i in range(j, nt)]
    i_tab = jnp.asarray([p[0] for p in pairs], jnp.int32)
    j_tab = jnp.asarray([p[1] for p in pairs], jnp.int32)
    wide, narrow = FOX_HPS * PACK, FOX_HPS * HEAD_DIM
    qs = pl.BlockSpec((FOX_T, wide), lambda hp, t, it, jt: (it[t], hp))
    ks = pl.BlockSpec((FOX_T, wide), lambda hp, t, it, jt: (jt[t], hp))
    whole = pl.BlockSpec((S, narrow), lambda hp, t, it, jt: (0, hp))
    cs = pl.BlockSpec((FOX_T, narrow), lambda hp, t, it, jt: (jt[t], hp))
    rs = pl.BlockSpec((FOX_HPS, 1, S), lambda hp, t, it, jt: (hp, 0, 0))
    shp = jax.ShapeDtypeStruct((S, FOX_WIDTH), _CD)
    grid_spec = pltpu.PrefetchScalarGridSpec(
        num_scalar_prefetch=2, grid=(N_FOX_HEADS // FOX_HPS, len(pairs)), in_specs=[qs, ks, ks, qs],
        out_specs=[whole, cs, cs, cs, rs],
        scratch_shapes=[pltpu.VMEM((S, wide), F32), pltpu.VMEM((FOX_T, wide), F32),
                        pltpu.VMEM((FOX_T, wide), F32), pltpu.VMEM((FOX_HPS, FOX_T, 1), F32),
                        pltpu.VMEM((FOX_HPS, 1, S), F32)])
    return _pcall(body, name=name, grid_spec=grid_spec,
                  out_shape=[shp, shp, shp, jax.ShapeDtypeStruct((S, FOX_WIDTH), F32),
                             jax.ShapeDtypeStruct((N_FOX_HEADS, 1, S), F32)],
                  compiler_params=_params("parallel", "arbitrary"))(i_tab, j_tab, qp, kp, vp, dop)


def _layer_step(x, tgt, w, p, late_weights=None, grad_sink=None, after=None, first_weights=None):
    S = x.shape[0]
    after_norm, after_proj = after if after is not None else (None, None)
    h = _rms_fwd(x, p["norm_mix_g"], name="rms_mix", after=after_norm)
    if first_weights is not None:
        w = {**w, **first_weights(h)}
    qkv = _mm(h, w["qkv"][:, 3 * DIL_WIDTH:], name="proj_fox", out_dtype=_CD, tn=768, tm=2048, after=after_proj)
    dil_qkv = _proj_dil(h, w["qkv"], name="proj_dil")
    zf = _mm(h, w["f"], name="proj_f")
    gl = _mm(h, w["g"], name="proj_gate", tn=1024, out_dtype=_CD)

    dil_o, dil_l = [], []
    for g in range(N_DIL_GROUPS):
        og, lg = _dil_fwd(dil_qkv[g], g, name=f"dil_fwd{g}")
        dil_o.append(og), dil_l.append(lg)
    o_a = _dil_mix_fwd(dil_o, dil_l, name="dil_mix")

    c = _fox_cumsum(zf, p["b_fgt"], name="fox_cumsum")
    fqp, fkp, fvp = _fox_pack_fwd(qkv, c, name="fox_pack")
    o_b, flse = _fox_fwd(fqp, fkp, fvp, name="fox_fwd")

    if late_weights is not None:
        w = {**w, **late_weights(o_b)}
    y_a = _mm(o_a, w["dil_out"], name="y_a", tn=1024, out_dtype=_CD)
    y_b = _mm(o_b, w["fox_out"], name="y_b", tn=1024, out_dtype=_CD)
    merged, x1, h2 = _gated_mix_out(gl, p["b_gate"], y_a, y_b, w["out"], x, p["norm_ffn_g"], name="mix_out")
    gate, up, act = _ffn_in_act(h2, w["ffn_in"], name="ffn_in")
    loss, dx2, dg_final = _ffn_down_loss(act, w["ffn_down"], x1, p["norm_final_g"], tgt, name="ffn_down_loss")

    gw_ffn_down = _mm(act, dx2, name="gw_ffn_down", ta=True, out_dtype=_CD, tm=1408)
    dgu = _d_swiglu(dx2, w["ffn_down"], gate, up, name="d_swiglu")
    gw_ffn_in = _mm(h2, dgu, name="gw_ffn_in", ta=True, out_dtype=_CD, tn=1408, out_blocks=1408, b_halves=True)
    sink = grad_sink if grad_sink is not None else (lambda group, grads: None)
    tok = sink("ffn", dict(ffn_in=gw_ffn_in, ffn_down=gw_ffn_down))
    dx1, dg_ffn = _mm(dgu, w["ffn_in"], name="d_h2", tb=True, tk=1408, b_blocks=True, tm=1024, a_halves=True,
                      rms_bwd=(x1, p["norm_ffn_g"], dx2), after=tok)

    gw_out = _mm(merged, dx1, name="gw_out", ta=True, out_dtype=_CD)
    dy_a, dy_b, dgl, db_gate = _gate_bwd(dx1, w["out"], gl, p["b_gate"], y_a, y_b, name="gate_bwd")
    do_a = _mm(dy_a, w["dil_out"], name="d_o_a", tb=True)
    gw_dil_out = _mm(o_a, dy_a, name="gw_dil_out", ta=True, out_dtype=_CD, tn=1024)
    do_b = _mm(dy_b, w["fox_out"], name="d_o_b", tb=True)
    gw_fox_out = _mm(o_b, dy_b, name="gw_fox_out", ta=True, out_dtype=_CD, tn=1024)
    tok = sink("mix", dict(dil_out=gw_dil_out, fox_out=gw_fox_out, out=gw_out))

    bqp, bdop = _fox_pack_bwd(qkv, c, o_b, flse, do_b, name="fox_pack_bwd", after=tok)
    dqp, dkp, dvp, dck, dcq = _fox_bwd(bqp, fkp, fvp, bdop, name="fox_bwd")
    dc = dcq[:, 0, :].T - dck.reshape(S, N_FOX_HEADS, HEAD_DIM)[:, :, 0]
    dc = jnp.pad(dc, ((0, 0), (0, F_PAD - N_FOX_HEADS)))
    dzf, db_fgt = _fox_cumsum_bwd(dc, zf, p["b_fgt"], name="fox_cumsum_bwd")

    douts = _dil_mix_bwd(do_a, dil_o, dil_l, name="dil_mix_bwd", after=tok)
    dqs, dks, dvs = [], [], []
    for g in range(N_DIL_GROUPS):
        dq, dk, dv = _dil_bwd(dil_qkv[g], douts[3 + g], douts[g], g, name=f"dil_bwd{g}")
        for parts, t in ((dqs, dq), (dks, dk), (dvs, dv)):
            parts.extend([t[0].astype(_CD), t[1].astype(_CD)])
    dqkv = jnp.concatenate(dqs + dks + dvs + [dqp, dkp, dvp], axis=1)

    gw_qkv = _mm(h, dqkv, name="gw_qkv", ta=True, out_dtype=_CD, tn=768)
    gw_g = _mm(h, dgl, name="gw_gate", ta=True, out_dtype=_CD)
    gw_f = _mm(h, dzf, name="gw_f", ta=True, out_dtype=_CD)
    tok = sink("in", dict(qkv=gw_qkv, f=gw_f, g=gw_g))
    dx, dg_mix = _mm(dqkv, w["qkv"], name="d_h", tb=True, tk=QKV_COLS, tm=256, more=((dgl, w["g"]), (dzf, w["f"])),
                     rms_bwd=(x, p["norm_mix_g"], dx1), after=tok)

    gw = dict(qkv=gw_qkv, f=gw_f, g=gw_g, dil_out=gw_dil_out, fox_out=gw_fox_out, out=gw_out, ffn_in=gw_ffn_in,
              ffn_down=gw_ffn_down)
    small = dict(norm_mix_g=dg_mix, b_fgt=db_fgt, b_gate=db_gate, norm_ffn_g=dg_ffn, norm_final_g=dg_final)
    return loss, dx, gw, small


def _position():
    return lax.axis_index("x"), lax.axis_index("y"), lax.axis_index("c")


def _other_chips(x, y):
    return [(1 - x, y), (x, 1 - y), (1 - x, 1 - y)]


ROW_TILE = 16


def _row_chunks(rows, want=4):
    n = want
    while n > 1 and rows % (n * ROW_TILE):
        n //= 2
    return n


SEM_SPEC = pl.BlockSpec(memory_space=pltpu.SEMAPHORE)
ANY_SPEC = pl.BlockSpec(memory_space=pl.ANY)
DATAFLOW = pltpu.SideEffectType.DATAFLOW_SIDE_EFFECTING


def _in_hbm(a):
    return pltpu.with_memory_space_constraint(a, pltpu.HBM)


def _split_copy_start(srcs, land_shapes, copies, after, *, name):
    n, m = len(srcs), len(land_shapes)

    def body(*refs):
        src_refs, land_refs = refs[:n], refs[n:n + m]
        send_sems, recv_sems = refs[n + m + 1], refs[n + m + 2]
        token = refs[-1]
        x, y, c = _position()
        for k, (src, dst, peer) in enumerate(copies(x, y, c, src_refs, land_refs)):
            pltpu.make_async_remote_copy(src_ref=src, dst_ref=dst, send_sem=send_sems.at[k], recv_sem=recv_sems.at[k],
                                         device_id=peer, device_id_type=MESH).start()
        token[...] = jnp.zeros_like(token)

    lands = [lax.empty(s.shape, s.dtype) for s in land_shapes]
    count = len(copies(0, 0, 0, srcs, lands))
    out = _pcall(
        body, name=name,
        out_shape=(pltpu.SemaphoreType.DMA((count,)), pltpu.SemaphoreType.DMA((count,)),
                   *[pltpu.HBM(s.shape, s.dtype) for s in srcs], *[pltpu.HBM(s.shape, s.dtype) for s in land_shapes],
                   jax.ShapeDtypeStruct((8, 128), F32)),
        in_specs=[HBM_SPEC] * (n + m) + [ANY_SPEC],
        out_specs=(SEM_SPEC, SEM_SPEC, *[HBM_SPEC] * (n + m), pl.BlockSpec(memory_space=pltpu.VMEM)),
        input_output_aliases={k: 2 + k for k in range(n + m)},
        compiler_params=pltpu.CompilerParams(has_side_effects=DATAFLOW),
    )(*[_in_hbm(s) for s in srcs], *[_in_hbm(l) for l in lands], after)
    return out[0], out[1], list(out[2:2 + n]), list(out[2 + n:2 + n + m]), out[-1]


def _split_copy_wait(send_sems, recv_sems, srcs, lands, copies, after, *, name):
    n, m = len(srcs), len(lands)

    def body(*refs):
        src_refs, land_refs = refs[:n], refs[n:n + m]
        send, recv = refs[n + m], refs[n + m + 1]
        x, y, c = _position()
        for k, (src, dst, peer) in enumerate(copies(x, y, c, src_refs, land_refs)):
            cp = pltpu.make_async_remote_copy(src_ref=src, dst_ref=dst, send_sem=send.at[k], recv_sem=recv.at[k],
                                              device_id=peer, device_id_type=MESH)
            cp.wait_send()
            cp.wait_recv()

    afters = list(after) if isinstance(after, (list, tuple)) else [after]
    out = _pcall(
        body, name=name,
        out_shape=tuple(pltpu.HBM(s.shape, s.dtype) for s in list(srcs) + list(lands)),
        in_specs=[HBM_SPEC] * (n + m) + [SEM_SPEC, SEM_SPEC] + [ANY_SPEC] * len(afters),
        out_specs=tuple([HBM_SPEC] * (n + m)),
        input_output_aliases={k: k for k in range(n + m)},
        compiler_params=pltpu.CompilerParams(has_side_effects=DATAFLOW),
    )(*srcs, *lands, send_sems, recv_sems, *afters)
    return list(out[:n]), list(out[n:])


def _gather_copies(x, y, c, shard_refs, land_refs):
    out = []
    for s, l in zip(shard_refs, land_refs):
        half = s.shape[0] // 2
        nq = _row_chunks(half)
        for cx, cy in _other_chips(x, y):
            for q in range(nq):
                rows = pl.ds(c * half + q * (half // nq), half // nq)
                out.append((s.at[rows, :], l.at[2 * x + y, rows, :], (cx, cy, c)))
    return out


def _gather_whole_copies(x, y, c, shard_refs, land_refs):
    out = []
    for s, l in zip(shard_refs, land_refs):
        nq = _row_chunks(s.shape[0])
        for cx, cy in _other_chips(x, y):
            for q in range(nq):
                rows = pl.ds(q * (s.shape[0] // nq), s.shape[0] // nq)
                out.append((s.at[rows, :], l.at[2 * x + y, rows, :], (cx, cy, c)))
    return out


def _scatter_all_copies(x, y, c, block_refs, land_refs):
    out = []
    for g, l in zip(block_refs, land_refs):
        half = g.shape[1] // 2
        nq = _row_chunks(half)
        size = half // nq
        for q in range(nq):
            rows = pl.ds((1 - c) * half + q * size, size)
            out.append((g.at[2 * x + y, rows, :], l.at[0, pl.ds(q * size, size), :], (x, y, 1 - c)))
        for r, (cx, cy) in enumerate(_other_chips(x, y)):
            for j in range(2):
                h = c if j == 0 else 1 - c
                for q in range(nq):
                    rows = pl.ds(h * half + q * size, size)
                    out.append((g.at[2 * cx + cy, rows, :], l.at[1 + 2 * r + j, pl.ds(q * size, size), :], (cx, cy, h)))
    return out


def _forward_halves(lands, *, name):
    n = len(lands)

    def body(*refs):
        ins = refs[:n]
        send_sems, recv_sems = refs[2 * n:]
        x, y, c = _position()
        copies = []
        for w in range(n):
            half = ins[w].shape[1] // 2
            for r, (cx, cy) in enumerate(_other_chips(x, y)):
                blk = ins[w].at[2 * cx + cy, pl.ds(c * half, half), :]
                cp = pltpu.make_async_remote_copy(src_ref=blk, dst_ref=blk, send_sem=send_sems.at[w, r],
                                                  recv_sem=recv_sems.at[w, r], device_id=(x, y, 1 - c),
                                                  device_id_type=MESH)
                cp.start()
                copies.append(cp)
        for w in range(n):
            half = ins[w].shape[1] // 2
            for r, (cx, cy) in enumerate(_other_chips(x, y)):
                blk = ins[w].at[2 * cx + cy, pl.ds((1 - c) * half, half), :]
                pltpu.make_async_remote_copy(src_ref=blk, dst_ref=blk, send_sem=send_sems.at[w, r],
                                             recv_sem=recv_sems.at[w, r], device_id=(x, y, 1 - c),
                                             device_id_type=MESH).wait_recv()
        for cp in copies:
            cp.wait_send()

    return _pcall(
        body, name=name, in_specs=[HBM_SPEC] * n, out_specs=[HBM_SPEC] * n,
        out_shape=[jax.ShapeDtypeStruct(l.shape, l.dtype) for l in lands],
        input_output_aliases={k: k for k in range(n)},
        scratch_shapes=[pltpu.SemaphoreType.DMA((n, 3)), pltpu.SemaphoreType.DMA((n, 3))],
    )(*lands)


def _share_halves(halves, *, name):
    n = len(halves)

    def body(*refs):
        ins, outs = refs[:n], refs[n:2 * n]
        send_sems, recv_sems = refs[2 * n:]
        x, y, c = _position()
        copies = []
        for w in range(n):
            cp = pltpu.make_async_remote_copy(src_ref=ins[w], dst_ref=outs[w], send_sem=send_sems.at[w],
                                              recv_sem=recv_sems.at[w], device_id=(x, y, 1 - c), device_id_type=MESH)
            cp.start()
            copies.append(cp)
        for cp in copies:
            cp.wait()

    return _pcall(
        body, name=name, in_specs=[HBM_SPEC] * n, out_specs=[HBM_SPEC] * n,
        out_shape=[jax.ShapeDtypeStruct(h.shape, h.dtype) for h in halves],
        scratch_shapes=[pltpu.SemaphoreType.DMA((n,)), pltpu.SemaphoreType.DMA((n,))],
    )(*halves)


def _sum_small(part, after=None):
    rows, width = part.shape

    def body(x_ref, out_ref, all_ref, send_sems, recv_sems):
        x, y, c = _position()
        me, sibling = (x, y, c), (x, y, 1 - c)
        chips = _other_chips(x, y)

        def block(px, py, pc):
            return all_ref.at[pl.ds((4 * px + 2 * py + pc) * rows, rows), :]

        def copy(k, blk, to, src=None):
            return pltpu.make_async_remote_copy(
                src_ref=block(*blk) if src is None else src, dst_ref=block(*blk), send_sem=send_sems.at[k],
                recv_sem=recv_sems.at[k], device_id=to, device_id_type=MESH)

        all_ref[pl.ds((4 * x + 2 * y + c) * rows, rows), :] = x_ref[...]
        first = [copy(0, me, sibling, src=x_ref)]
        first += [copy(1 + j, me, (*chip, c), src=x_ref) for j, chip in enumerate(chips)]
        for cp in first:
            cp.start()
        passed = [copy(4 + j, (*chip, c), sibling) for j, chip in enumerate(chips)]
        for j, chip in enumerate(chips):
            copy(1 + j, (*chip, c), me).wait_recv()
            passed[j].start()
        copy(0, sibling, me).wait_recv()
        for j, chip in enumerate(chips):
            copy(4 + j, (*chip, 1 - c), me).wait_recv()
        for cp in first + passed:
            cp.wait_send()
        total = all_ref[0:rows, :]
        for d in range(1, 8):
            total = total + all_ref[d * rows:(d + 1) * rows, :]
        out_ref[...] = total

    vm = pl.BlockSpec(memory_space=pltpu.VMEM)
    return _pcall(
        body, after, name="sum_small", in_specs=[vm], out_specs=vm, out_shape=jax.ShapeDtypeStruct((rows, width), F32),
        scratch_shapes=[pltpu.VMEM((8 * rows, width), F32), pltpu.SemaphoreType.DMA((7,)), pltpu.SemaphoreType.DMA((7,))],
    )(part)


def _row_tile(R, C, itemsize=4, budget=1 << 20):
    for t in (512, 256, 128, 64, 32, 16, 8):
        if R % t == 0 and t * C * itemsize <= budget:
            return t
    return R


def _add_all(g, recv, where, *, name):
    _, R, C = g.shape
    half = R // 2
    t = _row_tile(half, C)
    nb = half // t

    def body(w_ref, g_ref, r_ref, o_ref):
        total = g_ref[0].astype(F32)
        for k in range(7):
            total = total + r_ref[k].astype(F32)
        o_ref[...] = total

    grid_spec = pltpu.PrefetchScalarGridSpec(
        num_scalar_prefetch=1, grid=(nb,),
        in_specs=[pl.BlockSpec((1, t, C), lambda i, wr: (wr[0], wr[1] * nb + i, 0)),
                  pl.BlockSpec((7, t, C), lambda i, wr: (0, i, 0))],
        out_specs=pl.BlockSpec((t, C), lambda i, wr: (i, 0)))
    return _pcall(body, name=name, grid_spec=grid_spec, out_shape=jax.ShapeDtypeStruct((half, C), F32),
                  compiler_params=_params("parallel"))(where, g, recv)


def _adamw(w, g, m, v, *, name):
    R, C = w.shape
    t = _row_tile(R, C)
    c1 = 1.0 - ADAM_B1 ** ADAM_STEP
    c2 = 1.0 - ADAM_B2 ** ADAM_STEP

    def body(w_ref, g_ref, m_ref, v_ref, d_ref, nm_ref, nv_ref):
        gv = g_ref[...]
        mn = ADAM_B1 * m_ref[...] + (1.0 - ADAM_B1) * gv
        vn = ADAM_B2 * v_ref[...] + (1.0 - ADAM_B2) * (gv * gv)
        d_ref[...] = -ADAM_LR * ((mn / c1) / (jnp.sqrt(vn / c2) + ADAM_EPS) + ADAM_WD * w_ref[...])
        nm_ref[...] = mn
        nv_ref[...] = vn

    blk = pl.BlockSpec((t, C), lambda i: (i, 0))
    shp = jax.ShapeDtypeStruct((R, C), F32)
    return _pcall(body, name=name, grid=(R // t,), in_specs=[blk] * 4, out_specs=[blk] * 3, out_shape=[shp] * 3,
                  compiler_params=_params("parallel"))(w, g, m, v)


def _adamw_halves(w, mine, theirs, m, v, core, *, name, after=None):
    R, C = w.shape
    half = R // 2
    t = _row_tile(half, C)
    nbh = half // t
    c1 = 1.0 - ADAM_B1 ** ADAM_STEP
    c2 = 1.0 - ADAM_B2 ** ADAM_STEP

    def body(core_ref, w_ref, a_ref, b_ref, m_ref, v_ref, *rest):
        g_ref, d_ref, nm_ref, nv_ref = rest[-4:]
        gv = jnp.where(pl.program_id(0) // nbh == core_ref[0], a_ref[...], b_ref[...])
        mn = ADAM_B1 * m_ref[...] + (1.0 - ADAM_B1) * gv
        vn = ADAM_B2 * v_ref[...] + (1.0 - ADAM_B2) * (gv * gv)
        g_ref[...] = gv
        d_ref[...] = -ADAM_LR * ((mn / c1) / (jnp.sqrt(vn / c2) + ADAM_EPS) + ADAM_WD * w_ref[...])
        nm_ref[...] = mn
        nv_ref[...] = vn

    blk = pl.BlockSpec((t, C), lambda i, cr: (i, 0))
    hblk = pl.BlockSpec((t, C), lambda i, cr: (i % nbh, 0))
    shp = jax.ShapeDtypeStruct((R, C), F32)
    tied = [] if after is None else [after]
    grid_spec = pltpu.PrefetchScalarGridSpec(num_scalar_prefetch=1, grid=(2 * nbh,),
                                             in_specs=[blk, hblk, hblk, blk, blk] + [ANY_SPEC] * len(tied),
                                             out_specs=[blk] * 4)
    return _pcall(body, name=name, grid_spec=grid_spec, out_shape=[shp] * 4,
                  compiler_params=_params("parallel"))(core, w, mine, theirs, m, v, *tied)


BIG = ("w_in", "w_dil_out", "w_fox_out", "w_out", "w_ffn_in", "w_ffn_down")
SMALL = ("norm_mix_g", "b_fgt", "b_gate", "norm_ffn_g", "norm_final_g")
ORDER = ("norm_mix_g", "w_in", "b_fgt", "b_gate", "w_dil_out", "w_fox_out", "w_out", "norm_ffn_g", "w_ffn_in",
         "w_ffn_down", "norm_final_g")
SMALL_ROWS = {"norm_mix_g": (0, 1), "b_gate": (1, 3), "norm_ffn_g": (3, 4), "norm_final_g": (4, 5), "b_fgt": (5, 6)}


def _columns_to_blocks(full, ncol):
    K = full.shape[0]
    return full.reshape(K, 4, ncol).transpose(1, 0, 2)


def _pieces_to_blocks(pieces, ncol):
    spans, start = [], 0
    for piece in pieces:
        spans.append((piece, start, start + piece.shape[1]))
        start += piece.shape[1]
    assert start == 4 * ncol
    blocks = []
    for k in range(4):
        lo, hi = k * ncol, (k + 1) * ncol
        parts = [p[:, max(lo, a) - a:min(hi, b) - a] for p, a, b in spans if a < hi and b > lo]
        blocks.append(parts[0] if len(parts) == 1 else jnp.concatenate(parts, axis=1))
    return jnp.stack(blocks)


def _blocks_to_pieces(blocks, widths):
    n, K, ncol = blocks.shape
    assert sum(widths) == n * ncol
    pieces, lo = [], 0
    for width in widths:
        hi = lo + width
        parts = [blocks[k][:, max(lo, k * ncol) - k * ncol:min(hi, (k + 1) * ncol) - k * ncol]
                 for k in range(n) if k * ncol < hi and (k + 1) * ncol > lo]
        pieces.append(parts[0] if len(parts) == 1 else jnp.concatenate(parts, axis=1))
        lo = hi
    return pieces


def _blocks_to_columns(blocks):
    n, K, ncol = blocks.shape
    return blocks.transpose(1, 0, 2).reshape(K, n * ncol)


def kernel(x, norm_mix_g, w_in, b_fgt, b_gate, w_dil_out, w_fox_out, w_out, norm_ffn_g, w_ffn_in, w_ffn_down, norm_final_g, loss_target, m_norm_mix_g, m_w_in, m_b_fgt, m_b_gate, m_w_dil_out, m_w_fox_out, m_w_out, m_norm_ffn_g, m_w_ffn_in, m_w_ffn_down, m_norm_final_g, v_norm_mix_g, v_w_in, v_b_fgt, v_b_gate, v_w_dil_out, v_w_fox_out, v_w_out, v_norm_ffn_g, v_w_ffn_in, v_w_ffn_down, v_norm_final_g):
    weights = dict(norm_mix_g=norm_mix_g, w_in=w_in, b_fgt=b_fgt, b_gate=b_gate, w_dil_out=w_dil_out,
                   w_fox_out=w_fox_out, w_out=w_out, norm_ffn_g=norm_ffn_g, w_ffn_in=w_ffn_in, w_ffn_down=w_ffn_down,
                   norm_final_g=norm_final_g)
    m_in = dict(norm_mix_g=m_norm_mix_g, w_in=m_w_in, b_fgt=m_b_fgt, b_gate=m_b_gate, w_dil_out=m_w_dil_out,
                w_fox_out=m_w_fox_out, w_out=m_w_out, norm_ffn_g=m_norm_ffn_g, w_ffn_in=m_w_ffn_in,
                w_ffn_down=m_w_ffn_down, norm_final_g=m_norm_final_g)
    v_in = dict(norm_mix_g=v_norm_mix_g, w_in=v_w_in, b_fgt=v_b_fgt, b_gate=v_b_gate, w_dil_out=v_w_dil_out,
                w_fox_out=v_w_fox_out, w_out=v_w_out, norm_ffn_g=v_norm_ffn_g, w_ffn_in=v_w_ffn_in,
                w_ffn_down=v_w_ffn_down, norm_final_g=v_norm_final_g)
    c = lax.axis_index("c")
    chip = 2 * lax.axis_index("x") + lax.axis_index("y")

    shards = {n: weights[n][0].astype(_CD) for n in BIG}
    in_shape = jax.ShapeDtypeStruct((4,) + shards["w_in"].shape, _CD)
    send_i, recv_i, in_src, in_land, token_in = _split_copy_start(
        [shards["w_in"]], [in_shape], _gather_copies, norm_mix_g, name="gather_in_start")
    late = BIG[1:]
    send_g, recv_g, late_src, late_land, token = _split_copy_start(
        [shards[n] for n in late], [jax.ShapeDtypeStruct((4,) + shards[n].shape, _CD) for n in late],
        _gather_whole_copies, token_in, name="gather_late_start")
    adam_in = [t[0] + token_in[0, 0] for t in (w_in, m_w_in, v_w_in)]
    p = dict(norm_mix_g=norm_mix_g, b_fgt=jnp.pad(b_fgt, ((0, 0), (0, F_PAD - N_FOX_HEADS))), b_gate=b_gate,
             norm_ffn_g=norm_ffn_g, norm_final_g=norm_final_g.reshape(1, D_MODEL))

    def first_weights(after):
        own, lands = _split_copy_wait(send_i, recv_i, in_src, in_land, _gather_copies, [after] + adam_in,
                                      name="gather_in_wait")
        (g_in,) = _forward_halves(lands, name="gather_in_forward")
        blocks = lax.dynamic_update_index_in_dim(g_in, own[0], chip, 0)
        qkv, f, g = _blocks_to_pieces(blocks, (QKV_COLS, N_FOX_HEADS, 2 * D_MODEL))
        return dict(qkv=qkv, f=jnp.pad(f, ((0, 0), (0, F_PAD - N_FOX_HEADS))), g=g)

    def late_weights(after):
        own, lands = _split_copy_wait(send_g, recv_g, late_src, late_land, _gather_whole_copies, after,
                                      name="gather_late_wait")
        g_dil, g_fox, g_out, g_ffn_in, g_ffn_down = [
            lax.dynamic_update_index_in_dim(l, s, chip, 0) for l, s in zip(lands, own)]
        return dict(dil_out=_blocks_to_columns(g_dil), fox_out=_blocks_to_columns(g_fox),
                    out=g_out.reshape(D_MODEL, D_MODEL), ffn_in=g_ffn_in,
                    ffn_down=g_ffn_down.reshape(D_FF, D_MODEL))

    def to_blocks(n, full):
        shape = weights[n].shape
        if full.ndim == 3:
            return full
        if n in ("w_out", "w_ffn_down"):
            return full.reshape(4, shape[1], shape[2])
        return _columns_to_blocks(full, shape[2])

    in_flight = {}

    def grad_sink(group, gw):
        if group == "in":
            named = {"w_in": _pieces_to_blocks([gw["qkv"], gw["f"][:, :N_FOX_HEADS], gw["g"]], weights["w_in"].shape[2])}
        else:
            named = {"w_" + k: v for k, v in gw.items()}
        srcs = [to_blocks(n, named[n]) for n in named]
        lands = [jax.ShapeDtypeStruct((7, s.shape[1] // 2, s.shape[2]), s.dtype) for s in srcs]
        started = _split_copy_start(srcs, lands, _scatter_all_copies, next(iter(gw.values())),
                                    name=f"scatter_{group}_start")
        in_flight[group] = (list(named), started)
        return started[-1]

    loss_part, grad_x, gw, small = _layer_step(x[0], loss_target[0], {}, p, late_weights, grad_sink,
                                               (token_in, token), first_weights)

    where = jnp.stack([chip, c]).astype(jnp.int32)

    def summed_halves(groups, after, name):
        halves = {}
        for group in groups:
            names, (send_s, recv_s, srcs, lands, _) = in_flight[group]
            srcs, recv = _split_copy_wait(send_s, recv_s, srcs, lands, _scatter_all_copies, after,
                                          name=f"scatter_{group}_wait")
            halves.update({n: _add_all(s, r, where, name=f"add_all_{n}") for n, s, r in zip(names, srcs, recv)})
        return {n: (h, o) for (n, h), o in zip(halves.items(), _share_halves(list(halves.values()), name=name))}

    grad_halves = summed_halves(("ffn", "mix"), grad_x, "share_halves")

    out_g, out_d, out_m, out_v = {}, {}, {}, {}
    core = jnp.reshape(c, (1,)).astype(jnp.int32)

    def adamw_big(n, after=None):
        shape = weights[n].shape
        wmv = adam_in if n == "w_in" else [t[0] for t in (weights[n], m_in[n], v_in[n])]
        mine, theirs = grad_halves[n]
        outs = _adamw_halves(wmv[0], mine, theirs, wmv[1], wmv[2], core, name=f"adamw_{n}", after=after)
        out_g[n], out_d[n], out_m[n], out_v[n] = [t.reshape(shape) for t in outs]

    early = ("w_dil_out", "w_fox_out", "w_out", "w_ffn_down")
    for n in early:
        adamw_big(n)

    packed = jnp.concatenate([
        small["norm_mix_g"], small["b_gate"].reshape(2, D_MODEL), small["norm_ffn_g"], small["norm_final_g"],
        jnp.pad(small["b_fgt"], ((0, 0), (0, D_MODEL - F_PAD))), jnp.pad(loss_part, ((0, 0), (0, D_MODEL - 1))),
        jnp.zeros((1, D_MODEL), F32)], axis=0)
    summed = _sum_small(packed, after=out_d[early[-1]])
    loss = summed[6, 0]

    grad_halves.update(summed_halves(("in",), [grad_x, summed] + [out_d[n] for n in early], "share_halves_in"))
    adamw_big("w_in")
    adamw_big("w_ffn_in", after=out_d["w_in"])

    for n in SMALL:
        lo, hi = SMALL_ROWS[n]
        shape = weights[n].shape
        g2 = summed[lo:hi].reshape(1, -1)[:, :weights[n].size]
        d2, m2, v2 = _adamw(weights[n].reshape(g2.shape), g2, m_in[n].reshape(g2.shape), v_in[n].reshape(g2.shape),
                            name=f"adamw_{n}")
        out_g[n], out_d[n], out_m[n], out_v[n] = [t.reshape(shape) for t in (g2, d2, m2, v2)]
    return (loss, grad_x[None], *[out_g[n] for n in ORDER], *[out_d[n] for n in ORDER],
            *[out_m[n] for n in ORDER], *[out_v[n] for n in ORDER])
```

```python
import numpy as np
import jax
import jax.numpy as jnp
from jax import lax
from jax.experimental import pallas as pl
from jax.experimental.pallas import tpu as pltpu

F32 = jnp.float32
_CD = jnp.bfloat16

D_MODEL = 1024
HEAD_DIM = 64
DIL_PAIRS = ((128, 1), (512, 4), (2048, 16))
N_DIL_GROUPS = 3
DIL_HEADS = 4
DIL_W = 128
DIL_OUT = DIL_HEADS * HEAD_DIM
DIL_WIDTH = N_DIL_GROUPS * DIL_OUT
N_FOX_HEADS = 8
FOX_WIDTH = N_FOX_HEADS * HEAD_DIM
D_FF = 2816
QKV_COLS = 3 * DIL_WIDTH + 3 * FOX_WIDTH
F_PAD = 128
RMS_EPS = 1e-6
NEG_INF = -1e30
ATTN_SCALE = HEAD_DIM ** -0.5
ADAM_LR, ADAM_B1, ADAM_B2, ADAM_EPS, ADAM_WD, ADAM_STEP = 0.001, 0.9, 0.999, 1e-08, 0.01, 10

VMEM_LIMIT = 48 * 1024 * 1024
VMEM_LIMIT_RESIDENT = 56 * 1024 * 1024
LANES = 128
MESH = pl.DeviceIdType.MESH
HBM_SPEC = pl.BlockSpec(memory_space=pltpu.HBM)


def _pcall(body, after=None, **kw):
    if after is None:
        return pl.pallas_call(body, **kw)
    n_in = len(kw["in_specs"])
    kw["in_specs"] = list(kw["in_specs"]) + [pl.BlockSpec(memory_space=pl.ANY)]

    def tied(*refs):
        return body(*refs[:n_in], *refs[n_in + 1:])

    call = pl.pallas_call(tied, **kw)
    return lambda *args: call(*args, after)


def _params(*sem):
    return pltpu.CompilerParams(dimension_semantics=sem, vmem_limit_bytes=VMEM_LIMIT)


def _pick(dim, pref):
    t = (min(pref, dim) // 128) * 128
    while t >= 128:
        if dim % t == 0:
            return t
        t -= 128
    return dim


def _rms_bwd_store(r, x_ref, g_ref, dres_ref, o_ref, dg_ref):
    xv = x_ref[...]
    rs = lax.rsqrt(jnp.mean(xv * xv, axis=-1, keepdims=True) + RMS_EPS)
    xh = xv * rs
    dxh = r * g_ref[...]
    o_ref[...] = dres_ref[...] + rs * (dxh - xh * jnp.mean(dxh * xh, axis=-1, keepdims=True))
    part = jnp.sum(r * xh, axis=0, keepdims=True)
    first = pl.program_id(0) == 0

    @pl.when(first)
    def _():
        dg_ref[...] = part

    @pl.when(jnp.logical_not(first))
    def _():
        dg_ref[...] += part


def _d_h2(dgu, w_blocks, x, g, dres, *, name, tm=256, after=None):
    _, S, F = dgu.shape
    n, D, C = w_blocks.shape
    assert n == 4 and F == 2 * C
    nt = (((1,), (1,)), ((), ()))

    def body(dg_ref, du_ref, w_ref, x_ref, g_ref, dres_ref, o_ref, dgn_ref):
        r = None
        for k in range(n):
            a_ref = dg_ref if k < 2 else du_ref
            p = lax.dot_general(a_ref[:, (k % 2) * C:(k % 2 + 1) * C].astype(_CD), w_ref[k].astype(_CD), nt,
                                preferred_element_type=F32)
            r = p if r is None else r + p
        _rms_bwd_store(r, x_ref, g_ref, dres_ref, o_ref, dgn_ref)

    row = pl.BlockSpec((tm, D), lambda i: (i, 0))
    vec = pl.BlockSpec((1, D), lambda i: (0, 0))
    return _pcall(
        body, after, name=name, grid=(S // tm,),
        in_specs=[pl.BlockSpec((None, tm, F), lambda i: (0, i, 0)), pl.BlockSpec((None, tm, F), lambda i: (1, i, 0)),
                  pl.BlockSpec((n, D, C), lambda i: (0, 0, 0)), row, vec, row],
        out_specs=[row, vec], out_shape=[jax.ShapeDtypeStruct((S, D), F32), jax.ShapeDtypeStruct((1, D), F32)],
        compiler_params=_params("arbitrary"))(dgu, dgu, w_blocks, x, g, dres)


def _mm(a, b, *, name, ta=False, tb=False, out_dtype=F32, add=None, tm=1024, tn=512, tk=2048, after=None,
        b_blocks=False, out_blocks=None, a_halves=False, b_halves=False, rms_bwd=None, more=()):
    if a_halves:
        M, K = a.shape[1], 2 * a.shape[2]
    elif ta:
        K, M = a.shape
    else:
        M, K = a.shape
    if b_halves:
        b_rows, b_cols = b.shape[1], 2 * b.shape[2]
    else:
        b_rows, b_cols = (b.shape[1], b.shape[0] * b.shape[2]) if b_blocks else b.shape
    if tb:
        N, K2 = b_rows, b_cols
    else:
        K2, N = b_rows, b_cols
    assert K == K2, (a.shape, b.shape)
    shard = b.shape[2] if b_blocks else None
    tm = _pick(M, tm)
    tn = _pick(shard if (b_blocks and not tb) else (out_blocks or N), tn)
    tk = _pick(shard if (b_blocks and tb) else K, tk)
    nk = K // tk
    dn = (((0 if ta else 1,), (1 if tb else 0,)), ((), ()))
    has_add = add is not None
    assert not (has_add and out_blocks)
    has_norm = rms_bwd is not None
    if has_norm:
        tn = N
        assert not out_blocks and out_dtype == F32
    assert not more or (nk == 1 and tb and not ta)

    def body(*refs):
        a_ref, b_ref = refs[0], refs[1]
        rest = list(refs[2:])
        more_refs = [(rest.pop(0), rest.pop(0)) for _ in more]
        add_ref = rest.pop(0) if has_add else None
        x_ref, g_ref, dres_ref = (rest.pop(0), rest.pop(0), rest.pop(0)) if has_norm else (None, None, None)
        o_ref = rest.pop(0)
        dg_ref = rest.pop(0) if has_norm else None
        bv = b_ref[0] if b_blocks else b_ref[...]
        p = lax.dot_general(a_ref[...].astype(_CD), bv.astype(_CD), dn, preferred_element_type=F32)
        for a2_ref, b2_ref in more_refs:
            p += lax.dot_general(a2_ref[...].astype(_CD), b2_ref[...].astype(_CD), dn, preferred_element_type=F32)

        def finish(r):
            if has_add:
                r = r + add_ref[...]
            if has_norm:
                _rms_bwd_store(r, x_ref, g_ref, dres_ref, o_ref, dg_ref)
            elif out_blocks:
                o_ref[0] = r.astype(out_dtype)
            else:
                o_ref[...] = r.astype(out_dtype)

        if nk == 1:
            finish(p)
        else:
            acc_ref = rest.pop(0)
            k = pl.program_id(2)

            @pl.when(k == 0)
            def _():
                acc_ref[...] = p

            @pl.when(k > 0)
            def _():
                acc_ref[...] += p

            @pl.when(k == nk - 1)
            def _():
                finish(acc_ref[...])

    if a_halves:
        ka = (K // 2) // tk
        a_spec = pl.BlockSpec((None, tm, tk), lambda i, j, k: (k // ka, i, k % ka))
    else:
        a_spec = pl.BlockSpec((tk, tm), lambda i, j, k: (k, i)) if ta else pl.BlockSpec((tm, tk), lambda i, j, k: (i, k))
    if b_halves:
        nb_ = (N // 2) // tn
        b_spec = pl.BlockSpec((None, tk, tn), lambda i, j, k: (j // nb_, k, j % nb_))
    elif b_blocks and tb:
        per = shard // tk
        b_spec = pl.BlockSpec((1, tn, tk), lambda i, j, k: (k // per, j, k % per))
    elif b_blocks:
        per = shard // tn
        b_spec = pl.BlockSpec((1, tk, tn), lambda i, j, k: (j // per, k, j % per))
    else:
        b_spec = pl.BlockSpec((tn, tk), lambda i, j, k: (j, k)) if tb else pl.BlockSpec((tk, tn), lambda i, j, k: (k, j))
    if out_blocks:
        oper = out_blocks // tn
        o_spec = pl.BlockSpec((1, tm, tn), lambda i, j, k: (j // oper, i, j % oper))
        out_shape = jax.ShapeDtypeStruct((N // out_blocks, M, out_blocks), out_dtype)
    else:
        o_spec = pl.BlockSpec((tm, tn), lambda i, j, k: (i, j))
        out_shape = jax.ShapeDtypeStruct((M, N), out_dtype)
    in_specs, args = [a_spec, b_spec], (a, b)
    for a2, b2 in more:
        assert a2.shape[0] == M and b2.shape == (N, a2.shape[1]), (a2.shape, b2.shape)
        in_specs += [pl.BlockSpec((tm, a2.shape[1]), lambda i, j, k: (i, 0)),
                     pl.BlockSpec((tn, a2.shape[1]), lambda i, j, k: (j, 0))]
        args += (a2, b2)
    if has_add:
        in_specs, args = in_specs + [o_spec], args + (add,)
    out_specs, semantics = o_spec, ("parallel", "parallel", "arbitrary")
    if has_norm:
        vec = pl.BlockSpec((1, N), lambda i, j, k: (0, 0))
        in_specs += [o_spec, vec, o_spec]
        args += tuple(rms_bwd)
        out_specs, out_shape = [o_spec, vec], [out_shape, jax.ShapeDtypeStruct((1, N), F32)]
        semantics = ("arbitrary", "arbitrary", "arbitrary")
    return _pcall(
        body, after, name=name, grid=(M // tm, N // tn, nk), in_specs=in_specs, out_specs=out_specs,
        out_shape=out_shape,
        scratch_shapes=[pltpu.VMEM((tm, tn), F32)] if nk > 1 else [],
        compiler_params=_params(*semantics),
    )(*args)


def _rms_fwd(x, g, *, name, tm=512, after=None):
    S, D = x.shape

    def body(x_ref, g_ref, h_ref):
        xv = x_ref[...]
        r = lax.rsqrt(jnp.mean(xv * xv, axis=-1, keepdims=True) + RMS_EPS)
        h_ref[...] = ((xv * r) * g_ref[...]).astype(h_ref.dtype)

    row = pl.BlockSpec((tm, D), lambda i: (i, 0))
    return _pcall(body, after, name=name, grid=(S // tm,), in_specs=[row, pl.BlockSpec((1, D), lambda i: (0, 0))],
                  out_specs=row, out_shape=jax.ShapeDtypeStruct((S, D), _CD), compiler_params=_params("parallel"))(x, g)


def _ffn_down_loss(act, w_down, x1, g, tgt, *, name, tm=512):
    S, D = x1.shape
    F = act.shape[1]

    def body(a_ref, b_ref, x_ref, g_ref, t_ref, loss_ref, dx_ref, dg_ref):
        xv = x_ref[...] + jnp.dot(a_ref[...].astype(_CD), b_ref[...].astype(_CD), preferred_element_type=F32)
        gv = g_ref[...]
        r = lax.rsqrt(jnp.mean(xv * xv, axis=-1, keepdims=True) + RMS_EPS)
        xh = xv * r
        err = xh * gv - t_ref[...]
        lpart = 0.5 * jnp.sum(jnp.mean(err * err, axis=-1, keepdims=True), axis=0, keepdims=True)
        dy = err * (1.0 / D)
        dxh = dy * gv
        dx_ref[...] = r * (dxh - xh * jnp.mean(dxh * xh, axis=-1, keepdims=True))
        gpart = jnp.sum(dy * xh, axis=0, keepdims=True)

        @pl.when(pl.program_id(0) == 0)
        def _():
            loss_ref[...] = lpart
            dg_ref[...] = gpart

        @pl.when(pl.program_id(0) > 0)
        def _():
            loss_ref[...] += lpart
            dg_ref[...] += gpart

    row = pl.BlockSpec((tm, D), lambda i: (i, 0))
    vec = pl.BlockSpec((1, D), lambda i: (0, 0))
    one = pl.BlockSpec((1, 1), lambda i: (0, 0))
    return _pcall(body, name=name, grid=(S // tm,),
                  in_specs=[pl.BlockSpec((tm, F), lambda i: (i, 0)), pl.BlockSpec((F, D), lambda i: (0, 0)), row, vec, row],
                  out_specs=[one, row, vec],
                  out_shape=[jax.ShapeDtypeStruct((1, 1), F32), jax.ShapeDtypeStruct((S, D), F32),
                             jax.ShapeDtypeStruct((1, D), F32)],
                  compiler_params=_params("arbitrary"))(act, w_down, x1, g, tgt)


def _sigmoid(z):
    return 1.0 / (1.0 + jnp.exp(-z))


def _gated_mix_out(gl, bg, ya, yb, w_out, x, g, *, name, tm=512):
    S, D = ya.shape

    def body(za_ref, zb_ref, ba_ref, bb_ref, ya_ref, yb_ref, w_ref, x_ref, g_ref, m_ref, x1_ref, h_ref):
        ga = _sigmoid(za_ref[...].astype(F32) + ba_ref[...])
        gb = _sigmoid(zb_ref[...].astype(F32) + bb_ref[...])
        merged = (ga * ya_ref[...].astype(F32) + gb * yb_ref[...].astype(F32)).astype(m_ref.dtype)
        m_ref[...] = merged
        x1 = x_ref[...] + jnp.dot(merged, w_ref[...].astype(_CD), preferred_element_type=F32)
        x1_ref[...] = x1
        rs = lax.rsqrt(jnp.mean(x1 * x1, axis=-1, keepdims=True) + RMS_EPS)
        h_ref[...] = ((x1 * rs) * g_ref[...]).astype(h_ref.dtype)

    lo = pl.BlockSpec((tm, D), lambda i: (i, 0))
    hi = pl.BlockSpec((tm, D), lambda i: (i, 1))
    vlo = pl.BlockSpec((1, D), lambda i: (0, 0))
    vhi = pl.BlockSpec((1, D), lambda i: (0, 1))
    whole = pl.BlockSpec((D, D), lambda i: (0, 0))
    return _pcall(body, name=name, grid=(S // tm,), in_specs=[lo, hi, vlo, vhi, lo, lo, whole, lo, vlo],
                  out_specs=[lo, lo, lo],
                  out_shape=[jax.ShapeDtypeStruct((S, D), _CD), jax.ShapeDtypeStruct((S, D), F32),
                             jax.ShapeDtypeStruct((S, D), _CD)],
                  compiler_params=_params("parallel"))(gl, gl, bg, bg, ya, yb, w_out, x, g)


def _gate_bwd(dx1, w_out, gl, bg, ya, yb, *, name, tm=512):
    S, D = ya.shape
    nt = (((1,), (1,)), ((), ()))

    def body(dx_ref, w_ref, za_ref, zb_ref, ba_ref, bb_ref, ya_ref, yb_ref, dya_ref, dyb_ref, dgl_ref, dbg_ref):
        dmv = lax.dot_general(dx_ref[...].astype(_CD), w_ref[...].astype(_CD), nt, preferred_element_type=F32)
        ga = _sigmoid(za_ref[...].astype(F32) + ba_ref[...])
        gb = _sigmoid(zb_ref[...].astype(F32) + bb_ref[...])
        dya_ref[...] = (dmv * ga).astype(dya_ref.dtype)
        dyb_ref[...] = (dmv * gb).astype(dyb_ref.dtype)
        dza = dmv * ya_ref[...].astype(F32) * ga * (1.0 - ga)
        dzb = dmv * yb_ref[...].astype(F32) * gb * (1.0 - gb)
        dgl_ref[:, :D] = dza.astype(dgl_ref.dtype)
        dgl_ref[:, D:] = dzb.astype(dgl_ref.dtype)
        pa = jnp.sum(dza, axis=0, keepdims=True)
        pb = jnp.sum(dzb, axis=0, keepdims=True)

        @pl.when(pl.program_id(0) == 0)
        def _():
            dbg_ref[:, :D] = pa
            dbg_ref[:, D:] = pb

        @pl.when(pl.program_id(0) > 0)
        def _():
            dbg_ref[:, :D] += pa
            dbg_ref[:, D:] += pb

    lo = pl.BlockSpec((tm, D), lambda i: (i, 0))
    hi = pl.BlockSpec((tm, D), lambda i: (i, 1))
    vlo = pl.BlockSpec((1, D), lambda i: (0, 0))
    vhi = pl.BlockSpec((1, D), lambda i: (0, 1))
    wide = pl.BlockSpec((tm, 2 * D), lambda i: (i, 0))
    vwide = pl.BlockSpec((1, 2 * D), lambda i: (0, 0))
    whole = pl.BlockSpec((D, D), lambda i: (0, 0))
    return _pcall(body, name=name, grid=(S // tm,), in_specs=[lo, whole, lo, hi, vlo, vhi, lo, lo],
                  out_specs=[lo, lo, wide, vwide],
                  out_shape=[jax.ShapeDtypeStruct((S, D), _CD), jax.ShapeDtypeStruct((S, D), _CD),
                             jax.ShapeDtypeStruct((S, 2 * D), _CD), jax.ShapeDtypeStruct((1, 2 * D), F32)],
                  compiler_params=_params("arbitrary"))(dx1, w_out, gl, gl, bg, bg, ya, yb)


def _ffn_in_act(h2, w_blocks, *, name, tm=512):
    S, D = h2.shape
    _, _, C = w_blocks.shape

    def body(a_ref, bg_ref, bu_ref, g_ref, u_ref, o_ref):
        av = a_ref[...].astype(_CD)
        gv = jnp.dot(av, bg_ref[0].astype(_CD), preferred_element_type=F32)
        uv = jnp.dot(av, bu_ref[0].astype(_CD), preferred_element_type=F32)
        g_ref[...] = gv.astype(g_ref.dtype)
        u_ref[...] = uv.astype(u_ref.dtype)
        o_ref[...] = (gv * _sigmoid(gv) * uv).astype(o_ref.dtype)

    out = pl.BlockSpec((tm, C), lambda i, j: (i, j))
    shp = jax.ShapeDtypeStruct((S, 2 * C), _CD)
    return _pcall(body, name=name, grid=(S // tm, 2),
                  in_specs=[pl.BlockSpec((tm, D), lambda i, j: (i, 0)), pl.BlockSpec((1, D, C), lambda i, j: (j, 0, 0)),
                            pl.BlockSpec((1, D, C), lambda i, j: (2 + j, 0, 0))],
                  out_specs=[out, out, out], out_shape=[shp, shp, shp],
                  compiler_params=_params("parallel", "arbitrary"))(h2, w_blocks, w_blocks)


def _d_swiglu(dx, w_down, gate, up, *, name, tm=512, tn=1408):
    S, D = dx.shape
    F = w_down.shape[0]
    nt = (((1,), (1,)), ((), ()))

    def body(a_ref, b_ref, g_ref, u_ref, o_ref):
        dv = lax.dot_general(a_ref[...].astype(_CD), b_ref[...].astype(_CD), nt, preferred_element_type=F32)
        gv = g_ref[...].astype(F32)
        sg = _sigmoid(gv)
        o_ref[0] = (dv * u_ref[...].astype(F32) * (sg * (1.0 + gv * (1.0 - sg)))).astype(o_ref.dtype)
        o_ref[1] = (dv * (gv * sg)).astype(o_ref.dtype)

    tile = pl.BlockSpec((tm, tn), lambda i, j: (i, j))
    return _pcall(body, name=name, grid=(S // tm, F // tn),
                  in_specs=[pl.BlockSpec((tm, D), lambda i, j: (i, 0)), pl.BlockSpec((tn, D), lambda i, j: (j, 0)),
                            tile, tile],
                  out_specs=pl.BlockSpec((2, tm, tn), lambda i, j: (0, i, j)),
                  out_shape=jax.ShapeDtypeStruct((2, S, F), _CD),
                  compiler_params=_params("parallel", "arbitrary"))(dx, w_down, gate, up)


def _split3(x):
    hi = x.astype(jnp.bfloat16)
    r1 = x - hi.astype(F32)
    mid = r1.astype(jnp.bfloat16)
    lo = (r1 - mid.astype(F32)).astype(jnp.bfloat16)
    return hi, mid, lo


def _ones_dot_left(ones, x):
    return sum(jnp.dot(ones, p, preferred_element_type=F32) for p in _split3(x))


def _ones_dot_right(x, ones):
    return sum(jnp.dot(p, ones, preferred_element_type=F32) for p in _split3(x))


def _head_sum(x):
    n = x.shape[1]
    r = lax.broadcasted_iota(jnp.int32, (n, n), 0) // HEAD_DIM
    c = lax.broadcasted_iota(jnp.int32, (n, n), 1) // HEAD_DIM
    return _ones_dot_right(x, (r == c).astype(jnp.bfloat16))


def _log_sigmoid(z):
    e = jnp.exp(-jnp.abs(z))
    t = 1.0 + e
    log1p_e = jnp.where(t == 1.0, e, jnp.log(t) * (e / jnp.where(t == 1.0, 1.0, t - 1.0)))
    return jnp.minimum(z, 0.0) - log1p_e


def _fox_cumsum(zf, bf, *, name):
    S, W = zf.shape
    nb = S // 128

    def body(z_ref, b_ref, c_ref):
        tri = (lax.broadcasted_iota(jnp.int32, (128, 128), 0) >= lax.broadcasted_iota(jnp.int32, (128, 128), 1))
        tri = tri.astype(jnp.bfloat16)

        def step(i, carry):
            rows = pl.ds(pl.multiple_of(i * 128, 128), 128)
            lf = _log_sigmoid(z_ref[rows, :] + b_ref[...])
            cb = _ones_dot_left(tri, lf) + carry
            c_ref[rows, :] = cb
            return cb[127:128, :]

        lax.fori_loop(0, nb, step, jnp.zeros((1, W), F32))

    return _pcall(body, name=name, out_shape=jax.ShapeDtypeStruct((S, W), F32),
                  compiler_params=pltpu.CompilerParams(vmem_limit_bytes=VMEM_LIMIT))(zf, bf)


def _fox_cumsum_bwd(dc, zf, bf, *, name):
    S, W = zf.shape
    nb = S // 128

    def body(dc_ref, z_ref, b_ref, dz_ref, db_ref):
        tri = (lax.broadcasted_iota(jnp.int32, (128, 128), 0) <= lax.broadcasted_iota(jnp.int32, (128, 128), 1))
        tri = tri.astype(jnp.bfloat16)

        def step(k, carry):
            tail, acc = carry
            i = nb - 1 - k
            rows = pl.ds(pl.multiple_of(i * 128, 128), 128)
            dlf = _ones_dot_left(tri, dc_ref[rows, :]) + tail
            dz = dlf * _sigmoid(-(z_ref[rows, :] + b_ref[...]))
            dz_ref[rows, :] = dz
            return dlf[0:1, :], acc + jnp.sum(dz, axis=0, keepdims=True)

        _, acc = lax.fori_loop(0, nb, step, (jnp.zeros((1, W), F32), jnp.zeros((1, W), F32)))
        db_ref[...] = acc

    return _pcall(body, name=name,
                  out_shape=[jax.ShapeDtypeStruct((S, W), F32), jax.ShapeDtypeStruct((1, W), F32)],
                  compiler_params=pltpu.CompilerParams(vmem_limit_bytes=VMEM_LIMIT))(dc, zf, bf)


def _proj_dil(h, w_qkv, *, name, tm=1024):
    S, D = h.shape
    tn = DIL_WIDTH

    def body(a_ref, b_ref, *rest):
        outs, acc = rest[:N_DIL_GROUPS], rest[N_DIL_GROUPS]
        prod = jnp.dot(a_ref[...].astype(_CD), b_ref[...].astype(_CD), preferred_element_type=F32)
        for k in range(tn // LANES):
            acc[k] = prod[:, k * LANES:(k + 1) * LANES]
        for g, (_, d) in enumerate(DIL_PAIRS):
            for half in range(DIL_OUT // LANES):
                k = g * (DIL_OUT // LANES) + half
                cols = slice(half * LANES, (half + 1) * LANES)
                for r in range(d):
                    rows = pl.ds(r, tm // d, stride=d) if d > 1 else slice(None)
                    outs[g][0, r, :, cols] = acc[k, rows, :].astype(outs[g].dtype)

    out_specs = [pl.BlockSpec((1, d, tm // d, DIL_OUT), lambda i, j: (j, 0, i, 0)) for _, d in DIL_PAIRS]
    out_shape = [jax.ShapeDtypeStruct((3, d, S // d, DIL_OUT), _CD) for _, d in DIL_PAIRS]
    outs = _pcall(body, name=name, grid=(S // tm, 3),
                  in_specs=[pl.BlockSpec((tm, D), lambda i, j: (i, 0)), pl.BlockSpec((D, tn), lambda i, j: (0, j))],
                  out_specs=out_specs, out_shape=out_shape, scratch_shapes=[pltpu.VMEM((tn // LANES, tm, LANES), F32)],
                  compiler_params=_params("parallel", "arbitrary"))(h, w_qkv)
    return [o.reshape(3, S, DIL_OUT) for o in outs]


def _dil_start(block, S, dilation):
    sub = S // dilation
    u0 = block * DIL_W
    return (u0 % sub) * dilation + u0 // sub


def _dil_slopes(group):
    h = np.arange(1, N_DIL_GROUPS * DIL_HEADS + 1, dtype=np.float32)
    s = (np.float32(2.0) ** (np.float32(-8.0) * h / np.float32(N_DIL_GROUPS * DIL_HEADS))).astype(np.float32)
    return [float(v) for v in s.reshape(N_DIL_GROUPS, DIL_HEADS)[group]]


def _dil_tiles(i, n, blocks_per_seq):
    qi = lax.broadcasted_iota(jnp.int32, (DIL_W, 2 * DIL_W), 0)
    kj = lax.broadcasted_iota(jnp.int32, (DIL_W, 2 * DIL_W), 1)
    rel = qi + DIL_W - kj
    first = ((4 * n + i) % blocks_per_seq) == 0
    valid = jnp.logical_and(jnp.logical_and(rel >= 0, rel <= DIL_W), jnp.logical_or(kj >= DIL_W, jnp.logical_not(first)))
    return valid, rel.astype(F32)


def _dil_window(cur_ref, prev_ref, i, cols):
    if i > 0:
        return cur_ref[(i - 1) * DIL_W:(i + 1) * DIL_W, cols]
    return jnp.concatenate([prev_ref[:, cols], cur_ref[:DIL_W, cols]], axis=0)


CHUNK = 4 * DIL_W


def _dil_rows(block, S, dilation):
    start = _dil_start(block, S, dilation)
    return pl.ds(start, DIL_W, stride=dilation) if dilation > 1 else pl.ds(start, DIL_W)


def SPLIT(S):
    return (DIL_OUT // LANES, S, LANES)


def _dil_fwd(qkv, group, *, name):
    S = qkv.shape[1]
    dilation = DIL_PAIRS[group][1]
    bps = (S // dilation) // DIL_W
    slopes = _dil_slopes(group)
    nt = (((1,), (1,)), ((), ()))

    def body(q_ref, k_ref, v_ref, kp_ref, vp_ref, on_ref, ln_ref, o_ref, l_ref):
        n = pl.program_id(0)
        for i in range(4):
            valid, rel = _dil_tiles(i, n, bps)
            rows = slice(i * DIL_W, (i + 1) * DIL_W)
            for h in range(DIL_HEADS):
                cols = slice(h * HEAD_DIM, (h + 1) * HEAD_DIM)
                qh = q_ref[rows, cols]
                k2, v2 = _dil_window(k_ref, kp_ref, i, cols), _dil_window(v_ref, vp_ref, i, cols)
                s = lax.dot_general(qh, k2, nt, preferred_element_type=F32) * ATTN_SCALE - (slopes[h] * dilation) * rel
                s = jnp.where(valid, s, NEG_INF)
                m = jnp.max(s, axis=-1, keepdims=True)
                p = jnp.exp(s - m)
                den = jnp.sum(p, axis=-1, keepdims=True)
                acc = jnp.dot(p.astype(_CD), v2, preferred_element_type=F32)
                o_ref[rows, cols] = acc / den
                l_ref[rows, cols] = jnp.broadcast_to(m + jnp.log(den), (DIL_W, HEAD_DIM))
        for i in range(4):
            rows = slice(i * DIL_W, (i + 1) * DIL_W)
            nat = _dil_rows(4 * n + i, S, dilation)
            for half in range(DIL_OUT // LANES):
                cols = slice(half * LANES, (half + 1) * LANES)
                on_ref[half, nat, :] = o_ref[rows, cols]
                ln_ref[half, nat, :] = l_ref[rows, cols]

    def cur(which):
        return pl.BlockSpec((None, CHUNK, DIL_OUT), lambda n: (which, n, 0))

    def prev(which):
        return pl.BlockSpec((None, DIL_W, DIL_OUT), lambda n: (which, jnp.maximum(4 * n - 1, 0), 0))

    whole = pl.BlockSpec(SPLIT(S), lambda n: (0, 0, 0))
    return _pcall(body, name=name, grid=(S // CHUNK,), in_specs=[cur(0), cur(1), cur(2), prev(1), prev(2)],
                  out_specs=[whole, whole],
                  out_shape=[jax.ShapeDtypeStruct(SPLIT(S), F32), jax.ShapeDtypeStruct(SPLIT(S), F32)],
                  scratch_shapes=[pltpu.VMEM((CHUNK, DIL_OUT), F32), pltpu.VMEM((CHUNK, DIL_OUT), F32)],
                  compiler_params=_params("arbitrary"))(qkv, qkv, qkv, qkv, qkv)


STAT_OFFSET = HEAD_DIM // 2


def _dil_bwd(qkv, stats, do, group, *, name):
    S = qkv.shape[1]
    dilation = DIL_PAIRS[group][1]
    bps = (S // dilation) // DIL_W
    slopes = _dil_slopes(group)
    nchunk = S // CHUNK
    nt = (((1,), (1,)), ((), ()))
    tn = (((0,), (0,)), ((), ()))

    def body(q_ref, k_ref, v_ref, kp_ref, vp_ref, ln_ref, don_ref, dqn_ref, dkn_ref, dvn_ref,
             dk_s, dv_s, l_ref, do_ref, dq_ref):
        step = pl.program_id(0)
        n = nchunk - 1 - step
        for i in range(4):
            rows = slice(i * DIL_W, (i + 1) * DIL_W)
            nat = _dil_rows(4 * n + i, S, dilation)
            for half in range(DIL_OUT // LANES):
                cols = slice(half * LANES, (half + 1) * LANES)
                l_ref[rows, cols] = ln_ref[half, nat, :]
                do_ref[rows, cols] = don_ref[half, nat, :]

        @pl.when(step == 0)
        def _():
            dk_s[:, CHUNK:] = jnp.zeros((DIL_OUT, DIL_W), F32)
            dv_s[:, CHUNK:] = jnp.zeros((DIL_OUT, DIL_W), F32)

        dk_s[:, :CHUNK] = jnp.zeros((DIL_OUT, CHUNK), F32)
        dv_s[:, :CHUNK] = jnp.zeros((DIL_OUT, CHUNK), F32)
        for i in range(4):
            valid, rel = _dil_tiles(i, n, bps)
            rows = slice(i * DIL_W, (i + 1) * DIL_W)
            window = slice(i * DIL_W, (i + 2) * DIL_W)
            for h in range(DIL_HEADS):
                cols = slice(h * HEAD_DIM, (h + 1) * HEAD_DIM)
                qh = q_ref[rows, cols]
                k2, v2 = _dil_window(k_ref, kp_ref, i, cols), _dil_window(v_ref, vp_ref, i, cols)
                lh = l_ref[rows, h * HEAD_DIM:h * HEAD_DIM + 1]
                shift = l_ref[rows, h * HEAD_DIM + STAT_OFFSET:h * HEAD_DIM + STAT_OFFSET + 1]
                s = lax.dot_general(qh, k2, nt, preferred_element_type=F32) * ATTN_SCALE - (slopes[h] * dilation) * rel
                p = jnp.exp(jnp.where(valid, s, NEG_INF) - lh)
                dob = do_ref[rows, cols].astype(_CD)
                ds = p * (lax.dot_general(dob, v2, nt, preferred_element_type=F32) + shift)
                dsb = (ds * ATTN_SCALE).astype(_CD)
                dq_ref[rows, cols] = jnp.dot(dsb, k2, preferred_element_type=F32)
                dk_s[cols, window] += lax.dot_general(qh, dsb, tn, preferred_element_type=F32)
                dv_s[cols, window] += lax.dot_general(dob, p.astype(_CD), tn, preferred_element_type=F32)
        for i in range(4):
            rows = slice(i * DIL_W, (i + 1) * DIL_W)
            done = slice((i + 1) * DIL_W, (i + 2) * DIL_W)
            nat = _dil_rows(4 * n + i, S, dilation)
            dkb, dvb = dk_s[:, done].T, dv_s[:, done].T
            for half in range(DIL_OUT // LANES):
                cols = slice(half * LANES, (half + 1) * LANES)
                dqn_ref[half, nat, :] = dq_ref[rows, cols]
                dkn_ref[half, nat, :] = dkb[:, cols]
                dvn_ref[half, nat, :] = dvb[:, cols]
        dk_s[:, CHUNK:] = dk_s[:, :DIL_W]
        dv_s[:, CHUNK:] = dv_s[:, :DIL_W]

    def cur(which):
        return pl.BlockSpec((None, CHUNK, DIL_OUT), lambda s: (which, nchunk - 1 - s, 0))

    def prev(which):
        return pl.BlockSpec((None, DIL_W, DIL_OUT), lambda s: (which, jnp.maximum(4 * (nchunk - 1 - s) - 1, 0), 0))

    whole = pl.BlockSpec(SPLIT(S), lambda s: (0, 0, 0))
    shp = jax.ShapeDtypeStruct(SPLIT(S), F32)
    tile = pltpu.VMEM((CHUNK, DIL_OUT), F32)
    return _pcall(body, name=name, grid=(nchunk,),
                  in_specs=[cur(0), cur(1), cur(2), prev(1), prev(2), whole, whole],
                  out_specs=[whole, whole, whole], out_shape=[shp, shp, shp],
                  scratch_shapes=[pltpu.VMEM((DIL_OUT, CHUNK + DIL_W), F32), pltpu.VMEM((DIL_OUT, CHUNK + DIL_W), F32),
                                  tile, tile, tile],
                  compiler_params=pltpu.CompilerParams(dimension_semantics=("arbitrary",),
                                                       vmem_limit_bytes=VMEM_LIMIT_RESIDENT))(
        qkv, qkv, qkv, qkv, qkv, stats, do)


def _dil_mix_fwd(os_, ls_, *, name, tm=512):
    nh, S, _ = os_[0].shape

    def body(o0, o1, o2, l0, l1, l2, out_ref):
        for half in range(nh):
            ls = [l0[half], l1[half], l2[half]]
            m = jnp.maximum(jnp.maximum(ls[0], ls[1]), ls[2])
            es = [jnp.exp(l - m) for l in ls]
            den = es[0] + es[1] + es[2]
            mixed = (es[0] * o0[half] + es[1] * o1[half] + es[2] * o2[half]) / den
            out_ref[:, half * LANES:(half + 1) * LANES] = mixed.astype(out_ref.dtype)

    halves = pl.BlockSpec((nh, tm, LANES), lambda i: (0, i, 0))
    row = pl.BlockSpec((tm, nh * LANES), lambda i: (i, 0))
    return _pcall(body, name=name, grid=(S // tm,), in_specs=[halves] * 6, out_specs=row,
                  out_shape=jax.ShapeDtypeStruct((S, nh * LANES), _CD), compiler_params=_params("parallel"))(*os_, *ls_)


def _dil_mix_bwd(doa, os_, ls_, *, name, tm=512, after=None):
    nh, S, _ = os_[0].shape

    def body(d_ref, o0, o1, o2, l0, l1, l2, do0, do1, do2, st0, st1, st2):
        first = lax.broadcasted_iota(jnp.int32, (tm, LANES), 1) % HEAD_DIM < STAT_OFFSET
        for half in range(nh):
            dv = d_ref[:, half * LANES:(half + 1) * LANES]
            ls = [l0[half], l1[half], l2[half]]
            m = jnp.maximum(jnp.maximum(ls[0], ls[1]), ls[2])
            es = [jnp.exp(l - m) for l in ls]
            den = es[0] + es[1] + es[2]
            al = [e / den for e in es]
            da = [_head_sum(dv * o[half]) for o in (o0, o1, o2)]
            mean = al[0] * da[0] + al[1] * da[1] + al[2] * da[2]
            for a, l, do_ref, st_ref in zip(al, ls, (do0, do1, do2), (st0, st1, st2)):
                do_ref[half] = a * dv
                st_ref[half] = jnp.where(first, l, -a * mean)

    halves = pl.BlockSpec((nh, tm, LANES), lambda i: (0, i, 0))
    row = pl.BlockSpec((tm, nh * LANES), lambda i: (i, 0))
    shp = jax.ShapeDtypeStruct((nh, S, LANES), F32)
    return _pcall(body, after, name=name, grid=(S // tm,), in_specs=[row] + [halves] * 6, out_specs=[halves] * 6,
                  out_shape=[shp] * 6, compiler_params=_params("parallel"))(doa, *os_, *ls_)


FOX_T = 512


PACK = 2 * HEAD_DIM
HEAD_PAIRS = N_FOX_HEADS // 2
FOX_HPS = 8
Q_BLOCK0 = 0
K_BLOCK0 = FOX_WIDTH // PACK
V_BLOCK0 = 2 * FOX_WIDTH // PACK


def _pieces(x):
    hi = x.astype(jnp.bfloat16).astype(F32)
    r = x - hi
    mid = r.astype(jnp.bfloat16).astype(F32)
    lo = (r - mid).astype(jnp.bfloat16).astype(F32)
    return [hi, mid, lo]


def _extras(first, second, rows):
    lane = lax.broadcasted_iota(jnp.int32, (rows, HEAD_DIM), 1)
    out = jnp.zeros((rows, HEAD_DIM), F32)
    for base, triple in ((0, first), (3, second)):
        if all(isinstance(v, float) for v in triple) and len(set(triple)) == 1:
            if triple[0] != 0.0:
                out = jnp.where(jnp.logical_and(lane >= base, lane < base + 3), triple[0], out)
        else:
            for idx, val in enumerate(triple):
                out = jnp.where(lane == base + idx, val, out)
    return out


def _head_column(c, h):
    lane = lax.broadcasted_iota(jnp.int32, c.shape, 1)
    return jnp.sum(jnp.where(lane == h, c, 0.0), axis=1, keepdims=True)


ONES3 = [1.0, 1.0, 1.0]
ZEROS3 = [0.0, 0.0, 0.0]


def _fox_pack_fwd(qkv, c, *, name, tm=1024):
    S = qkv.shape[0]

    def body(q_ref, k_ref, v_ref, c_ref, qo_ref, ko_ref, vo_ref):
        hp = pl.program_id(1)
        cv = c_ref[...]
        v_extras = jnp.where(lax.broadcasted_iota(jnp.int32, (tm, HEAD_DIM), 1) < 3, 1.0, 0.0).astype(vo_ref.dtype)
        for hh in range(2):
            ch = _pieces(_head_column(cv, 2 * hp + hh))
            src = slice(hh * HEAD_DIM, (hh + 1) * HEAD_DIM)
            lo = slice(hh * PACK, hh * PACK + HEAD_DIM)
            hi = slice(hh * PACK + HEAD_DIM, (hh + 1) * PACK)
            qo_ref[:, lo] = (q_ref[:, src].astype(F32) * ATTN_SCALE).astype(qo_ref.dtype)
            qo_ref[:, hi] = _extras(ch, ONES3, tm).astype(qo_ref.dtype)
            ko_ref[:, lo] = k_ref[:, src]
            ko_ref[:, hi] = _extras(ONES3, [-p for p in ch], tm).astype(ko_ref.dtype)
            vo_ref[:, lo] = v_ref[:, src]
            vo_ref[:, hi] = v_extras

    def src(block0):
        return pl.BlockSpec((tm, PACK), lambda i, hp: (i, block0 + hp))

    out = pl.BlockSpec((tm, 2 * PACK), lambda i, hp: (i, hp))
    shp = jax.ShapeDtypeStruct((S, N_FOX_HEADS * PACK), _CD)
    return _pcall(body, name=name, grid=(S // tm, HEAD_PAIRS),
                  in_specs=[src(Q_BLOCK0), src(K_BLOCK0), src(V_BLOCK0), pl.BlockSpec((tm, PACK), lambda i, hp: (i, 0))],
                  out_specs=[out, out, out], out_shape=[shp, shp, shp],
                  compiler_params=_params("parallel", "parallel"))(qkv, qkv, qkv, c)


def _fox_fwd(qp, kp, vp, *, name):
    S = qp.shape[0]
    nt = S // FOX_T
    nt_dims = (((1,), (1,)), ((), ()))
    tn_dims = (((0,), (0,)), ((), ()))

    def body(i_tab, j_tab, q_ref, k_ref, v_ref, o_ref, l_ref, m_s, acc_s):
        t = pl.program_id(1)
        i, j = i_tab[t], j_tab[t]

        @pl.when(j == 0)
        def _():
            m_s[...] = jnp.full((FOX_HPS, 1, FOX_T), NEG_INF, F32)
            acc_s[...] = jnp.zeros((FOX_HPS, PACK, FOX_T), F32)

        def tile(diagonal):
            for hh in range(FOX_HPS):
                cols = slice(hh * PACK, (hh + 1) * PACK)
                st = lax.dot_general(k_ref[:, cols], q_ref[:, cols], nt_dims, preferred_element_type=F32)
                if diagonal:
                    key = lax.broadcasted_iota(jnp.int32, (FOX_T, FOX_T), 0)
                    qry = lax.broadcasted_iota(jnp.int32, (FOX_T, FOX_T), 1)
                    st = jnp.where(key <= qry, st, NEG_INF)
                m_old = m_s[hh]
                m_new = jnp.maximum(m_old, jnp.max(st, axis=0, keepdims=True))
                pt = jnp.exp(st - m_new)
                acc_s[hh] = jnp.exp(m_old - m_new) * acc_s[hh] + lax.dot_general(
                    v_ref[:, cols], pt.astype(_CD), tn_dims, preferred_element_type=F32)
                m_s[hh] = m_new

        @pl.when(j < i)
        def _():
            tile(False)

        @pl.when(j == i)
        def _():
            tile(True)
            for hh in range(FOX_HPS):
                acc = acc_s[hh]
                den = acc[HEAD_DIM:HEAD_DIM + 1, :]
                cols = slice(hh * HEAD_DIM, (hh + 1) * HEAD_DIM)
                o_ref[:, cols] = (acc[:HEAD_DIM, :] / den).T
                l_ref[:, cols] = jnp.broadcast_to(m_s[hh] + jnp.log(den), (HEAD_DIM, FOX_T)).T

    pairs = [(i, j) for i in range(nt) for j in range(i + 1)]
    i_tab = jnp.asarray([p[0] for p in pairs], jnp.int32)
    j_tab = jnp.asarray([p[1] for p in pairs], jnp.int32)
    qs = pl.BlockSpec((FOX_T, FOX_HPS * PACK), lambda hp, t, it, jt: (it[t], hp))
    ks = pl.BlockSpec((FOX_T, FOX_HPS * PACK), lambda hp, t, it, jt: (jt[t], hp))
    os_ = pl.BlockSpec((FOX_T, FOX_HPS * HEAD_DIM), lambda hp, t, it, jt: (it[t], hp))
    shp = jax.ShapeDtypeStruct((S, FOX_WIDTH), F32)
    grid_spec = pltpu.PrefetchScalarGridSpec(
        num_scalar_prefetch=2, grid=(N_FOX_HEADS // FOX_HPS, len(pairs)), in_specs=[qs, ks, ks], out_specs=[os_, os_],
        scratch_shapes=[pltpu.VMEM((FOX_HPS, 1, FOX_T), F32), pltpu.VMEM((FOX_HPS, PACK, FOX_T), F32)])
    return _pcall(body, name=name, grid_spec=grid_spec, out_shape=[shp, shp],
                  compiler_params=_params("parallel", "arbitrary"))(i_tab, j_tab, qp, kp, vp)


def _fox_pack_bwd(qkv, c, o, lse, do, *, name, tm=1024, after=None):
    S = qkv.shape[0]

    def body(q_ref, c_ref, o_ref, l_ref, do_ref, qo_ref, do_out_ref):
        hp = pl.program_id(1)
        cv = c_ref[...]
        for hh in range(2):
            src = slice(hh * HEAD_DIM, (hh + 1) * HEAD_DIM)
            lo = slice(hh * PACK, hh * PACK + HEAD_DIM)
            hi = slice(hh * PACK + HEAD_DIM, (hh + 1) * PACK)
            shift = _head_column(cv, 2 * hp + hh) - l_ref[:, hh * HEAD_DIM:hh * HEAD_DIM + 1]
            dov = do_ref[:, src]
            dsum = jnp.sum(dov * o_ref[:, src], axis=-1, keepdims=True)
            qo_ref[:, lo] = (q_ref[:, src].astype(F32) * ATTN_SCALE).astype(qo_ref.dtype)
            qo_ref[:, hi] = _extras(_pieces(shift), ONES3, tm).astype(qo_ref.dtype)
            do_out_ref[:, lo] = dov.astype(do_out_ref.dtype)
            do_out_ref[:, hi] = _extras(_pieces(-dsum), ZEROS3, tm).astype(do_out_ref.dtype)

    pair = pl.BlockSpec((tm, PACK), lambda i, hp: (i, hp))
    out = pl.BlockSpec((tm, 2 * PACK), lambda i, hp: (i, hp))
    shp = jax.ShapeDtypeStruct((S, N_FOX_HEADS * PACK), _CD)
    return _pcall(body, after, name=name, grid=(S // tm, HEAD_PAIRS),
                  in_specs=[pl.BlockSpec((tm, PACK), lambda i, hp: (i, Q_BLOCK0 + hp)),
                            pl.BlockSpec((tm, PACK), lambda i, hp: (i, 0)), pair, pair, pair],
                  out_specs=[out, out], out_shape=[shp, shp],
                  compiler_params=_params("parallel", "parallel"))(qkv, c, o, lse, do)


def _fox_bwd(qp, kp, vp, dop, *, name):
    S = qp.shape[0]
    nt = S // FOX_T
    nt_dims = (((1,), (1,)), ((), ()))
    tn_dims = (((0,), (0,)), ((), ()))

    def body(i_tab, j_tab, q_ref, k_ref, v_ref, do_ref, dq_ref, dk_ref, dv_ref, dc_ref, dr_ref,
             dq_s, dk_s, dv_s, dc_s, dr_s):
        t = pl.program_id(1)
        i, j = i_tab[t], j_tab[t]

        @pl.when(t == 0)
        def _():
            dq_s[...] = jnp.zeros((S, FOX_HPS * PACK), F32)
            dr_s[...] = jnp.zeros((FOX_HPS, 1, S), F32)

        @pl.when(i == j)
        def _():
            dk_s[...] = jnp.zeros((FOX_T, FOX_HPS * PACK), F32)
            dv_s[...] = jnp.zeros((FOX_T, FOX_HPS * PACK), F32)
            dc_s[...] = jnp.zeros((FOX_HPS, FOX_T, 1), F32)

        def tile(diagonal):
            rows = pl.ds(pl.multiple_of(i * FOX_T, FOX_T), FOX_T)
            for hh in range(FOX_HPS):
                cols = slice(hh * PACK, (hh + 1) * PACK)
                qv, kv, vv, dov = q_ref[:, cols], k_ref[:, cols], v_ref[:, cols], do_ref[:, cols]
                pt = jnp.exp(lax.dot_general(kv, qv, nt_dims, preferred_element_type=F32))
                if diagonal:
                    key = lax.broadcasted_iota(jnp.int32, (FOX_T, FOX_T), 0)
                    qry = lax.broadcasted_iota(jnp.int32, (FOX_T, FOX_T), 1)
                    pt = jnp.where(key <= qry, pt, 0.0)
                dst = pt * lax.dot_general(vv, dov, nt_dims, preferred_element_type=F32)
                dsb = dst.astype(_CD)
                dc_s[hh] += jnp.sum(dst, axis=1, keepdims=True)
                dr_s[hh, :, rows] += jnp.sum(dst, axis=0, keepdims=True)
                dv_s[:, cols] += jnp.dot(pt.astype(_CD), dov, preferred_element_type=F32)
                dk_s[:, cols] += jnp.dot(dsb, qv, preferred_element_type=F32)
                dq_s[rows, cols] += lax.dot_general(dsb, kv, tn_dims, preferred_element_type=F32)

        @pl.when(i > j)
        def _():
            tile(False)

        @pl.when(i == j)
        def _():
            tile(True)

        @pl.when(i == nt - 1)
        def _():
            for hh in range(FOX_HPS):
                src = slice(hh * PACK, hh * PACK + HEAD_DIM)
                dst_cols = slice(hh * HEAD_DIM, (hh + 1) * HEAD_DIM)
                dk_ref[:, dst_cols] = dk_s[:, src].astype(dk_ref.dtype)
                dv_ref[:, dst_cols] = dv_s[:, src].astype(dv_ref.dtype)
                dc_ref[:, dst_cols] = jnp.broadcast_to(dc_s[hh], (FOX_T, HEAD_DIM))

        @pl.when(t == len(pairs) - 1)
        def _():
            for hh in range(FOX_HPS):
                dq_ref[:, hh * HEAD_DIM:(hh + 1) * HEAD_DIM] = (
                    dq_s[:, hh * PACK:hh * PACK + HEAD_DIM] * ATTN_SCALE).astype(dq_ref.dtype)
            dr_ref[...] = dr_s[...]

    pairs = [(i, j) for j in range(nt) for i in range(j, nt)]
    i_tab = jnp.asarray([p[0] for p in pairs], jnp.int32)
    j_tab = jnp.asarray([p[1] for p in pairs], jnp.int32)
    wide, narrow = FOX_HPS * PACK, FOX_HPS * HEAD_DIM
    qs = pl.BlockSpec((FOX_T, wide), lambda hp, t, it, jt: (it[t], hp))
    ks = pl.BlockSpec((FOX_T, wide), lambda hp, t, it, jt: (jt[t], hp))
    whole = pl.BlockSpec((S, narrow), lambda hp, t, it, jt: (0, hp))
    cs = pl.BlockSpec((FOX_T, narrow), lambda hp, t, it, jt: (jt[t], hp))
    rs = pl.BlockSpec((FOX_HPS, 1, S), lambda hp, t, it, jt: (hp, 0, 0))
    shp = jax.ShapeDtypeStruct((S, FOX_WIDTH), _CD)
    grid_spec = pltpu.PrefetchScalarGridSpec(
        num_scalar_prefetch=2, grid=(N_FOX_HEADS // FOX_HPS, len(pairs)), in_specs=[qs, ks, ks, qs],
        out_specs=[whole, cs, cs, cs, rs],
        scratch_shapes=[pltpu.VMEM((S, wide), F32), pltpu.VMEM((FOX_T, wide), F32),
                        pltpu.VMEM((FOX_T, wide), F32), pltpu.VMEM((FOX_HPS, FOX_T, 1), F32),
                        pltpu.VMEM((FOX_HPS, 1, S), F32)])
    return _pcall(body, name=name, grid_spec=grid_spec,
                  out_shape=[shp, shp, shp, jax.ShapeDtypeStruct((S, FOX_WIDTH), F32),
                             jax.ShapeDtypeStruct((N_FOX_HEADS, 1, S), F32)],
                  compiler_params=_params("parallel", "arbitrary"))(i_tab, j_tab, qp, kp, vp, dop)


def _layer_step(x, tgt, w, p, late_weights=None, grad_sink=None, after=None, first_weights=None):
    S = x.shape[0]
    after_norm, after_proj = after if after is not None else (None, None)
    h = _rms_fwd(x, p["norm_mix_g"], name="rms_mix", after=after_norm)
    if first_weights is not None:
        w = {**w, **first_weights(h)}
    qkv = _mm(h, w["qkv"][:, 3 * DIL_WIDTH:], name="proj_fox", out_dtype=_CD, tn=768, tm=2048, after=after_proj)
    dil_qkv = _proj_dil(h, w["qkv"], name="proj_dil")
    zf = _mm(h, w["f"], name="proj_f")
    gl = _mm(h, w["g"], name="proj_gate", tn=1024, out_dtype=_CD)

    dil_o, dil_l = [], []
    for g in range(N_DIL_GROUPS):
        og, lg = _dil_fwd(dil_qkv[g], g, name=f"dil_fwd{g}")
        dil_o.append(og), dil_l.append(lg)
    o_a = _dil_mix_fwd(dil_o, dil_l, name="dil_mix")

    c = _fox_cumsum(zf, p["b_fgt"], name="fox_cumsum")
    fqp, fkp, fvp = _fox_pack_fwd(qkv, c, name="fox_pack")
    o_b, flse = _fox_fwd(fqp, fkp, fvp, name="fox_fwd")

    if late_weights is not None:
        w = {**w, **late_weights(o_b)}
    y_a = _mm(o_a, w["dil_out"], name="y_a", tn=1024, out_dtype=_CD)
    y_b = _mm(o_b, w["fox_out"], name="y_b", tn=1024, out_dtype=_CD)
    merged, x1, h2 = _gated_mix_out(gl, p["b_gate"], y_a, y_b, w["out"], x, p["norm_ffn_g"], name="mix_out")
    gate, up, act = _ffn_in_act(h2, w["ffn_in"], name="ffn_in")
    loss, dx2, dg_final = _ffn_down_loss(act, w["ffn_down"], x1, p["norm_final_g"], tgt, name="ffn_down_loss")

    gw_ffn_down = _mm(act, dx2, name="gw_ffn_down", ta=True, out_dtype=_CD, tm=1408)
    dgu = _d_swiglu(dx2, w["ffn_down"], gate, up, name="d_swiglu")
    gw_ffn_in = _mm(h2, dgu, name="gw_ffn_in", ta=True, out_dtype=_CD, tn=1408, out_blocks=1408, b_halves=True)
    sink = grad_sink if grad_sink is not None else (lambda group, grads: None)
    tok = sink("ffn", dict(ffn_in=gw_ffn_in, ffn_down=gw_ffn_down))
    dx1, dg_ffn = _d_h2(dgu, w["ffn_in"], x1, p["norm_ffn_g"], dx2, name="d_h2", after=tok)

    gw_out = _mm(merged, dx1, name="gw_out", ta=True, out_dtype=_CD)
    dy_a, dy_b, dgl, db_gate = _gate_bwd(dx1, w["out"], gl, p["b_gate"], y_a, y_b, name="gate_bwd")
    do_a = _mm(dy_a, w["dil_out"], name="d_o_a", tb=True)
    gw_dil_out = _mm(o_a, dy_a, name="gw_dil_out", ta=True, out_dtype=_CD, tn=1024)
    do_b = _mm(dy_b, w["fox_out"], name="d_o_b", tb=True)
    gw_fox_out = _mm(o_b, dy_b, name="gw_fox_out", ta=True, out_dtype=_CD, tn=1024)
    tok = sink("mix", dict(dil_out=gw_dil_out, fox_out=gw_fox_out, out=gw_out))

    bqp, bdop = _fox_pack_bwd(qkv, c, o_b, flse, do_b, name="fox_pack_bwd", after=tok)
    dqp, dkp, dvp, dck, dcq = _fox_bwd(bqp, fkp, fvp, bdop, name="fox_bwd")
    dc = dcq[:, 0, :].T - dck.reshape(S, N_FOX_HEADS, HEAD_DIM)[:, :, 0]
    dc = jnp.pad(dc, ((0, 0), (0, F_PAD - N_FOX_HEADS)))
    dzf, db_fgt = _fox_cumsum_bwd(dc, zf, p["b_fgt"], name="fox_cumsum_bwd")

    douts = _dil_mix_bwd(do_a, dil_o, dil_l, name="dil_mix_bwd", after=tok)
    dqs, dks, dvs = [], [], []
    for g in range(N_DIL_GROUPS):
        dq, dk, dv = _dil_bwd(dil_qkv[g], douts[3 + g], douts[g], g, name=f"dil_bwd{g}")
        for parts, t in ((dqs, dq), (dks, dk), (dvs, dv)):
            parts.extend([t[0].astype(_CD), t[1].astype(_CD)])
    dqkv = jnp.concatenate(dqs + dks + dvs + [dqp, dkp, dvp], axis=1)

    gw_qkv = _mm(h, dqkv, name="gw_qkv", ta=True, out_dtype=_CD, tn=768)
    gw_g = _mm(h, dgl, name="gw_gate", ta=True, out_dtype=_CD)
    gw_f = _mm(h, dzf, name="gw_f", ta=True, out_dtype=_CD)
    tok = sink("in", dict(qkv=gw_qkv, f=gw_f, g=gw_g))
    dx, dg_mix = _mm(dqkv, w["qkv"], name="d_h", tb=True, tk=QKV_COLS, tm=256, more=((dgl, w["g"]), (dzf, w["f"])),
                     rms_bwd=(x, p["norm_mix_g"], dx1), after=tok)

    gw = dict(qkv=gw_qkv, f=gw_f, g=gw_g, dil_out=gw_dil_out, fox_out=gw_fox_out, out=gw_out, ffn_in=gw_ffn_in,
              ffn_down=gw_ffn_down)
    small = dict(norm_mix_g=dg_mix, b_fgt=db_fgt, b_gate=db_gate, norm_ffn_g=dg_ffn, norm_final_g=dg_final)
    return loss, dx, gw, small


def _position():
    return lax.axis_index("x"), lax.axis_index("y"), lax.axis_index("c")


def _other_chips(x, y):
    return [(1 - x, y), (x, 1 - y), (1 - x, 1 - y)]


ROW_TILE = 16


def _row_chunks(rows, want=4):
    n = want
    while n > 1 and rows % (n * ROW_TILE):
        n //= 2
    return n


SEM_SPEC = pl.BlockSpec(memory_space=pltpu.SEMAPHORE)
ANY_SPEC = pl.BlockSpec(memory_space=pl.ANY)
DATAFLOW = pltpu.SideEffectType.DATAFLOW_SIDE_EFFECTING


def _in_hbm(a):
    return pltpu.with_memory_space_constraint(a, pltpu.HBM)


def _split_copy_start(srcs, land_shapes, copies, after, *, name):
    n, m = len(srcs), len(land_shapes)

    def body(*refs):
        src_refs, land_refs = refs[:n], refs[n:n + m]
        send_sems, recv_sems = refs[n + m + 1], refs[n + m + 2]
        token = refs[-1]
        x, y, c = _position()
        for k, (src, dst, peer) in enumerate(copies(x, y, c, src_refs, land_refs)):
            pltpu.make_async_remote_copy(src_ref=src, dst_ref=dst, send_sem=send_sems.at[k], recv_sem=recv_sems.at[k],
                                         device_id=peer, device_id_type=MESH).start()
        token[...] = jnp.zeros_like(token)

    lands = [lax.empty(s.shape, s.dtype) for s in land_shapes]
    count = len(copies(0, 0, 0, srcs, lands))
    out = _pcall(
        body, name=name,
        out_shape=(pltpu.SemaphoreType.DMA((count,)), pltpu.SemaphoreType.DMA((count,)),
                   *[pltpu.HBM(s.shape, s.dtype) for s in srcs], *[pltpu.HBM(s.shape, s.dtype) for s in land_shapes],
                   jax.ShapeDtypeStruct((8, 128), F32)),
        in_specs=[HBM_SPEC] * (n + m) + [ANY_SPEC],
        out_specs=(SEM_SPEC, SEM_SPEC, *[HBM_SPEC] * (n + m), pl.BlockSpec(memory_space=pltpu.VMEM)),
        input_output_aliases={k: 2 + k for k in range(n + m)},
        compiler_params=pltpu.CompilerParams(has_side_effects=DATAFLOW),
    )(*[_in_hbm(s) for s in srcs], *[_in_hbm(l) for l in lands], after)
    return out[0], out[1], list(out[2:2 + n]), list(out[2 + n:2 + n + m]), out[-1]


def _split_copy_wait(send_sems, recv_sems, srcs, lands, copies, after, *, name):
    n, m = len(srcs), len(lands)

    def body(*refs):
        src_refs, land_refs = refs[:n], refs[n:n + m]
        send, recv = refs[n + m], refs[n + m + 1]
        x, y, c = _position()
        for k, (src, dst, peer) in enumerate(copies(x, y, c, src_refs, land_refs)):
            cp = pltpu.make_async_remote_copy(src_ref=src, dst_ref=dst, send_sem=send.at[k], recv_sem=recv.at[k],
                                              device_id=peer, device_id_type=MESH)
            cp.wait_send()
            cp.wait_recv()

    afters = list(after) if isinstance(after, (list, tuple)) else [after]
    out = _pcall(
        body, name=name,
        out_shape=tuple(pltpu.HBM(s.shape, s.dtype) for s in list(srcs) + list(lands)),
        in_specs=[HBM_SPEC] * (n + m) + [SEM_SPEC, SEM_SPEC] + [ANY_SPEC] * len(afters),
        out_specs=tuple([HBM_SPEC] * (n + m)),
        input_output_aliases={k: k for k in range(n + m)},
        compiler_params=pltpu.CompilerParams(has_side_effects=DATAFLOW),
    )(*srcs, *lands, send_sems, recv_sems, *afters)
    return list(out[:n]), list(out[n:])


def _gather_copies(x, y, c, shard_refs, land_refs):
    out = []
    for s, l in zip(shard_refs, land_refs):
        half = s.shape[0] // 2
        nq = _row_chunks(half)
        for cx, cy in _other_chips(x, y):
            for q in range(nq):
                rows = pl.ds(c * half + q * (half // nq), half // nq)
                out.append((s.at[rows, :], l.at[2 * x + y, rows, :], (cx, cy, c)))
    return out


def _gather_whole_copies(x, y, c, shard_refs, land_refs):
    out = []
    for s, l in zip(shard_refs, land_refs):
        nq = _row_chunks(s.shape[0])
        for cx, cy in _other_chips(x, y):
            for q in range(nq):
                rows = pl.ds(q * (s.shape[0] // nq), s.shape[0] // nq)
                out.append((s.at[rows, :], l.at[2 * x + y, rows, :], (cx, cy, c)))
    return out


def _scatter_all_copies(x, y, c, block_refs, land_refs):
    out = []
    for g, l in zip(block_refs, land_refs):
        half = g.shape[1] // 2
        nq = _row_chunks(half)
        size = half // nq
        for q in range(nq):
            rows = pl.ds((1 - c) * half + q * size, size)
            out.append((g.at[2 * x + y, rows, :], l.at[0, pl.ds(q * size, size), :], (x, y, 1 - c)))
        for r, (cx, cy) in enumerate(_other_chips(x, y)):
            for j in range(2):
                h = c if j == 0 else 1 - c
                for q in range(nq):
                    rows = pl.ds(h * half + q * size, size)
                    out.append((g.at[2 * cx + cy, rows, :], l.at[1 + 2 * r + j, pl.ds(q * size, size), :], (cx, cy, h)))
    return out


def _forward_halves(lands, *, name):
    n = len(lands)

    def body(*refs):
        ins = refs[:n]
        send_sems, recv_sems = refs[2 * n:]
        x, y, c = _position()
        copies = []
        for w in range(n):
            half = ins[w].shape[1] // 2
            for r, (cx, cy) in enumerate(_other_chips(x, y)):
                blk = ins[w].at[2 * cx + cy, pl.ds(c * half, half), :]
                cp = pltpu.make_async_remote_copy(src_ref=blk, dst_ref=blk, send_sem=send_sems.at[w, r],
                                                  recv_sem=recv_sems.at[w, r], device_id=(x, y, 1 - c),
                                                  device_id_type=MESH)
                cp.start()
                copies.append(cp)
        for w in range(n):
            half = ins[w].shape[1] // 2
            for r, (cx, cy) in enumerate(_other_chips(x, y)):
                blk = ins[w].at[2 * cx + cy, pl.ds((1 - c) * half, half), :]
                pltpu.make_async_remote_copy(src_ref=blk, dst_ref=blk, send_sem=send_sems.at[w, r],
                                             recv_sem=recv_sems.at[w, r], device_id=(x, y, 1 - c),
                                             device_id_type=MESH).wait_recv()
        for cp in copies:
            cp.wait_send()

    return _pcall(
        body, name=name, in_specs=[HBM_SPEC] * n, out_specs=[HBM_SPEC] * n,
        out_shape=[jax.ShapeDtypeStruct(l.shape, l.dtype) for l in lands],
        input_output_aliases={k: k for k in range(n)},
        scratch_shapes=[pltpu.SemaphoreType.DMA((n, 3)), pltpu.SemaphoreType.DMA((n, 3))],
    )(*lands)


def _share_halves(halves, *, name):
    n = len(halves)

    def body(*refs):
        ins, outs = refs[:n], refs[n:2 * n]
        send_sems, recv_sems = refs[2 * n:]
        x, y, c = _position()
        copies = []
        for w in range(n):
            cp = pltpu.make_async_remote_copy(src_ref=ins[w], dst_ref=outs[w], send_sem=send_sems.at[w],
                                              recv_sem=recv_sems.at[w], device_id=(x, y, 1 - c), device_id_type=MESH)
            cp.start()
            copies.append(cp)
        for cp in copies:
            cp.wait()

    return _pcall(
        body, name=name, in_specs=[HBM_SPEC] * n, out_specs=[HBM_SPEC] * n,
        out_shape=[jax.ShapeDtypeStruct(h.shape, h.dtype) for h in halves],
        scratch_shapes=[pltpu.SemaphoreType.DMA((n,)), pltpu.SemaphoreType.DMA((n,))],
    )(*halves)


def _sum_small(part, after=None):
    rows, width = part.shape

    def body(x_ref, out_ref, all_ref, send_sems, recv_sems):
        x, y, c = _position()
        me, sibling = (x, y, c), (x, y, 1 - c)
        chips = _other_chips(x, y)

        def block(px, py, pc):
            return all_ref.at[pl.ds((4 * px + 2 * py + pc) * rows, rows), :]

        def copy(k, blk, to, src=None):
            return pltpu.make_async_remote_copy(
                src_ref=block(*blk) if src is None else src, dst_ref=block(*blk), send_sem=send_sems.at[k],
                recv_sem=recv_sems.at[k], device_id=to, device_id_type=MESH)

        all_ref[pl.ds((4 * x + 2 * y + c) * rows, rows), :] = x_ref[...]
        first = [copy(0, me, sibling, src=x_ref)]
        first += [copy(1 + j, me, (*chip, c), src=x_ref) for j, chip in enumerate(chips)]
        for cp in first:
            cp.start()
        passed = [copy(4 + j, (*chip, c), sibling) for j, chip in enumerate(chips)]
        for j, chip in enumerate(chips):
            copy(1 + j, (*chip, c), me).wait_recv()
            passed[j].start()
        copy(0, sibling, me).wait_recv()
        for j, chip in enumerate(chips):
            copy(4 + j, (*chip, 1 - c), me).wait_recv()
        for cp in first + passed:
            cp.wait_send()
        total = all_ref[0:rows, :]
        for d in range(1, 8):
            total = total + all_ref[d * rows:(d + 1) * rows, :]
        out_ref[...] = total

    vm = pl.BlockSpec(memory_space=pltpu.VMEM)
    return _pcall(
        body, after, name="sum_small", in_specs=[vm], out_specs=vm, out_shape=jax.ShapeDtypeStruct((rows, width), F32),
        scratch_shapes=[pltpu.VMEM((8 * rows, width), F32), pltpu.SemaphoreType.DMA((7,)), pltpu.SemaphoreType.DMA((7,))],
    )(part)


def _row_tile(R, C, itemsize=4, budget=1 << 20):
    for t in (512, 256, 128, 64, 32, 16, 8):
        if R % t == 0 and t * C * itemsize <= budget:
            return t
    return R


def _add_all(g, recv, where, *, name):
    _, R, C = g.shape
    half = R // 2
    t = _row_tile(half, C)
    nb = half // t

    def body(w_ref, g_ref, r_ref, o_ref):
        total = g_ref[0].astype(F32)
        for k in range(7):
            total = total + r_ref[k].astype(F32)
        o_ref[...] = total

    grid_spec = pltpu.PrefetchScalarGridSpec(
        num_scalar_prefetch=1, grid=(nb,),
        in_specs=[pl.BlockSpec((1, t, C), lambda i, wr: (wr[0], wr[1] * nb + i, 0)),
                  pl.BlockSpec((7, t, C), lambda i, wr: (0, i, 0))],
        out_specs=pl.BlockSpec((t, C), lambda i, wr: (i, 0)))
    return _pcall(body, name=name, grid_spec=grid_spec, out_shape=jax.ShapeDtypeStruct((half, C), F32),
                  compiler_params=_params("parallel"))(where, g, recv)


def _adamw(w, g, m, v, *, name):
    R, C = w.shape
    t = _row_tile(R, C)
    c1 = 1.0 - ADAM_B1 ** ADAM_STEP
    c2 = 1.0 - ADAM_B2 ** ADAM_STEP

    def body(w_ref, g_ref, m_ref, v_ref, d_ref, nm_ref, nv_ref):
        gv = g_ref[...]
        mn = ADAM_B1 * m_ref[...] + (1.0 - ADAM_B1) * gv
        vn = ADAM_B2 * v_ref[...] + (1.0 - ADAM_B2) * (gv * gv)
        d_ref[...] = -ADAM_LR * ((mn / c1) / (jnp.sqrt(vn / c2) + ADAM_EPS) + ADAM_WD * w_ref[...])
        nm_ref[...] = mn
        nv_ref[...] = vn

    blk = pl.BlockSpec((t, C), lambda i: (i, 0))
    shp = jax.ShapeDtypeStruct((R, C), F32)
    return _pcall(body, name=name, grid=(R // t,), in_specs=[blk] * 4, out_specs=[blk] * 3, out_shape=[shp] * 3,
                  compiler_params=_params("parallel"))(w, g, m, v)


def _adamw_halves(w, mine, theirs, m, v, core, *, name, after=None):
    R, C = w.shape
    half = R // 2
    t = _row_tile(half, C)
    nbh = half // t
    c1 = 1.0 - ADAM_B1 ** ADAM_STEP
    c2 = 1.0 - ADAM_B2 ** ADAM_STEP

    def body(core_ref, w_ref, a_ref, b_ref, m_ref, v_ref, *rest):
        g_ref, d_ref, nm_ref, nv_ref = rest[-4:]
        gv = jnp.where(pl.program_id(0) // nbh == core_ref[0], a_ref[...], b_ref[...])
        mn = ADAM_B1 * m_ref[...] + (1.0 - ADAM_B1) * gv
        vn = ADAM_B2 * v_ref[...] + (1.0 - ADAM_B2) * (gv * gv)
        g_ref[...] = gv
        d_ref[...] = -ADAM_LR * ((mn / c1) / (jnp.sqrt(vn / c2) + ADAM_EPS) + ADAM_WD * w_ref[...])
        nm_ref[...] = mn
        nv_ref[...] = vn

    blk = pl.BlockSpec((t, C), lambda i, cr: (i, 0))
    hblk = pl.BlockSpec((t, C), lambda i, cr: (i % nbh, 0))
    shp = jax.ShapeDtypeStruct((R, C), F32)
    tied = [] if after is None else [after]
    grid_spec = pltpu.PrefetchScalarGridSpec(num_scalar_prefetch=1, grid=(2 * nbh,),
                                             in_specs=[blk, hblk, hblk, blk, blk] + [ANY_SPEC] * len(tied),
                                             out_specs=[blk] * 4)
    return _pcall(body, name=name, grid_spec=grid_spec, out_shape=[shp] * 4,
                  compiler_params=_params("parallel"))(core, w, mine, theirs, m, v, *tied)


BIG = ("w_in", "w_dil_out", "w_fox_out", "w_out", "w_ffn_in", "w_ffn_down")
SMALL = ("norm_mix_g", "b_fgt", "b_gate", "norm_ffn_g", "norm_final_g")
ORDER = ("norm_mix_g", "w_in", "b_fgt", "b_gate", "w_dil_out", "w_fox_out", "w_out", "norm_ffn_g", "w_ffn_in",
         "w_ffn_down", "norm_final_g")
SMALL_ROWS = {"norm_mix_g": (0, 1), "b_gate": (1, 3), "norm_ffn_g": (3, 4), "norm_final_g": (4, 5), "b_fgt": (5, 6)}


def _columns_to_blocks(full, ncol):
    K = full.shape[0]
    return full.reshape(K, 4, ncol).transpose(1, 0, 2)


def _pieces_to_blocks(pieces, ncol):
    spans, start = [], 0
    for piece in pieces:
        spans.append((piece, start, start + piece.shape[1]))
        start += piece.shape[1]
    assert start == 4 * ncol
    blocks = []
    for k in range(4):
        lo, hi = k * ncol, (k + 1) * ncol
        parts = [p[:, max(lo, a) - a:min(hi, b) - a] for p, a, b in spans if a < hi and b > lo]
        blocks.append(parts[0] if len(parts) == 1 else jnp.concatenate(parts, axis=1))
    return jnp.stack(blocks)


def _blocks_to_pieces(blocks, widths):
    n, K, ncol = blocks.shape
    assert sum(widths) == n * ncol
    pieces, lo = [], 0
    for width in widths:
        hi = lo + width
        parts = [blocks[k][:, max(lo, k * ncol) - k * ncol:min(hi, (k + 1) * ncol) - k * ncol]
                 for k in range(n) if k * ncol < hi and (k + 1) * ncol > lo]
        pieces.append(parts[0] if len(parts) == 1 else jnp.concatenate(parts, axis=1))
        lo = hi
    return pieces


def _blocks_to_columns(blocks):
    n, K, ncol = blocks.shape
    return blocks.transpose(1, 0, 2).reshape(K, n * ncol)


def kernel(x, norm_mix_g, w_in, b_fgt, b_gate, w_dil_out, w_fox_out, w_out, norm_ffn_g, w_ffn_in, w_ffn_down, norm_final_g, loss_target, m_norm_mix_g, m_w_in, m_b_fgt, m_b_gate, m_w_dil_out, m_w_fox_out, m_w_out, m_norm_ffn_g, m_w_ffn_in, m_w_ffn_down, m_norm_final_g, v_norm_mix_g, v_w_in, v_b_fgt, v_b_gate, v_w_dil_out, v_w_fox_out, v_w_out, v_norm_ffn_g, v_w_ffn_in, v_w_ffn_down, v_norm_final_g):
    weights = dict(norm_mix_g=norm_mix_g, w_in=w_in, b_fgt=b_fgt, b_gate=b_gate, w_dil_out=w_dil_out,
                   w_fox_out=w_fox_out, w_out=w_out, norm_ffn_g=norm_ffn_g, w_ffn_in=w_ffn_in, w_ffn_down=w_ffn_down,
                   norm_final_g=norm_final_g)
    m_in = dict(norm_mix_g=m_norm_mix_g, w_in=m_w_in, b_fgt=m_b_fgt, b_gate=m_b_gate, w_dil_out=m_w_dil_out,
                w_fox_out=m_w_fox_out, w_out=m_w_out, norm_ffn_g=m_norm_ffn_g, w_ffn_in=m_w_ffn_in,
                w_ffn_down=m_w_ffn_down, norm_final_g=m_norm_final_g)
    v_in = dict(norm_mix_g=v_norm_mix_g, w_in=v_w_in, b_fgt=v_b_fgt, b_gate=v_b_gate, w_dil_out=v_w_dil_out,
                w_fox_out=v_w_fox_out, w_out=v_w_out, norm_ffn_g=v_norm_ffn_g, w_ffn_in=v_w_ffn_in,
                w_ffn_down=v_w_ffn_down, norm_final_g=v_norm_final_g)
    c = lax.axis_index("c")
    chip = 2 * lax.axis_index("x") + lax.axis_index("y")

    shards = {n: weights[n][0].astype(_CD) for n in BIG}
    in_shape = jax.ShapeDtypeStruct((4,) + shards["w_in"].shape, _CD)
    send_i, recv_i, in_src, in_land, token_in = _split_copy_start(
        [shards["w_in"]], [in_shape], _gather_copies, norm_mix_g, name="gather_in_start")
    late = BIG[1:]
    send_g, recv_g, late_src, late_land, token = _split_copy_start(
        [shards[n] for n in late], [jax.ShapeDtypeStruct((4,) + shards[n].shape, _CD) for n in late],
        _gather_whole_copies, token_in, name="gather_late_start")
    adam_in = [t[0] + token_in[0, 0] for t in (w_in, m_w_in, v_w_in)]
    p = dict(norm_mix_g=norm_mix_g, b_fgt=jnp.pad(b_fgt, ((0, 0), (0, F_PAD - N_FOX_HEADS))), b_gate=b_gate,
             norm_ffn_g=norm_ffn_g, norm_final_g=norm_final_g.reshape(1, D_MODEL))

    def first_weights(after):
        own, lands = _split_copy_wait(send_i, recv_i, in_src, in_land, _gather_copies, [after] + adam_in,
                                      name="gather_in_wait")
        (g_in,) = _forward_halves(lands, name="gather_in_forward")
        blocks = lax.dynamic_update_index_in_dim(g_in, own[0], chip, 0)
        qkv, f, g = _blocks_to_pieces(blocks, (QKV_COLS, N_FOX_HEADS, 2 * D_MODEL))
        return dict(qkv=qkv, f=jnp.pad(f, ((0, 0), (0, F_PAD - N_FOX_HEADS))), g=g)

    def late_weights(after):
        own, lands = _split_copy_wait(send_g, recv_g, late_src, late_land, _gather_whole_copies, after,
                                      name="gather_late_wait")
        g_dil, g_fox, g_out, g_ffn_in, g_ffn_down = [
            lax.dynamic_update_index_in_dim(l, s, chip, 0) for l, s in zip(lands, own)]
        return dict(dil_out=_blocks_to_columns(g_dil), fox_out=_blocks_to_columns(g_fox),
                    out=g_out.reshape(D_MODEL, D_MODEL), ffn_in=g_ffn_in,
                    ffn_down=g_ffn_down.reshape(D_FF, D_MODEL))

    def to_blocks(n, full):
        shape = weights[n].shape
        if full.ndim == 3:
            return full
        if n in ("w_out", "w_ffn_down"):
            return full.reshape(4, shape[1], shape[2])
        return _columns_to_blocks(full, shape[2])

    in_flight = {}

    def grad_sink(group, gw):
        if group == "in":
            named = {"w_in": _pieces_to_blocks([gw["qkv"], gw["f"][:, :N_FOX_HEADS], gw["g"]], weights["w_in"].shape[2])}
        else:
            named = {"w_" + k: v for k, v in gw.items()}
        srcs = [to_blocks(n, named[n]) for n in named]
        lands = [jax.ShapeDtypeStruct((7, s.shape[1] // 2, s.shape[2]), s.dtype) for s in srcs]
        started = _split_copy_start(srcs, lands, _scatter_all_copies, next(iter(gw.values())),
                                    name=f"scatter_{group}_start")
        in_flight[group] = (list(named), started)
        return started[-1]

    loss_part, grad_x, gw, small = _layer_step(x[0], loss_target[0], {}, p, late_weights, grad_sink,
                                               (token_in, token), first_weights)

    where = jnp.stack([chip, c]).astype(jnp.int32)

    def summed_halves(groups, after, name):
        halves = {}
        for group in groups:
            names, (send_s, recv_s, srcs, lands, _) = in_flight[group]
            srcs, recv = _split_copy_wait(send_s, recv_s, srcs, lands, _scatter_all_copies, after,
                                          name=f"scatter_{group}_wait")
            halves.update({n: _add_all(s, r, where, name=f"add_all_{n}") for n, s, r in zip(names, srcs, recv)})
        return {n: (h, o) for (n, h), o in zip(halves.items(), _share_halves(list(halves.values()), name=name))}

    grad_halves = summed_halves(("ffn", "mix"), grad_x, "share_halves")

    out_g, out_d, out_m, out_v = {}, {}, {}, {}
    core = jnp.reshape(c, (1,)).astype(jnp.int32)

    def adamw_big(n, after=None):
        shape = weights[n].shape
        wmv = adam_in if n == "w_in" else [t[0] for t in (weights[n], m_in[n], v_in[n])]
        mine, theirs = grad_halves[n]
        outs = _adamw_halves(wmv[0], mine, theirs, wmv[1], wmv[2], core, name=f"adamw_{n}", after=after)
        out_g[n], out_d[n], out_m[n], out_v[n] = [t.reshape(shape) for t in outs]

    early = ("w_dil_out", "w_fox_out", "w_out", "w_ffn_down")
    for n in early:
        adamw_big(n)
    grad_halves.update(summed_halves(("in",), [grad_x] + [out_d[n] for n in early], "share_halves_in"))
    adamw_big("w_in")

    packed = jnp.concatenate([
        small["norm_mix_g"], small["b_gate"].reshape(2, D_MODEL), small["norm_ffn_g"], small["norm_final_g"],
        jnp.pad(small["b_fgt"], ((0, 0), (0, D_MODEL - F_PAD))), jnp.pad(loss_part, ((0, 0), (0, D_MODEL - 1))),
        jnp.zeros((1, D_MODEL), F32)], axis=0)
    summed = _sum_small(packed, after=out_d["w_in"])
    loss = summed[6, 0]
    adamw_big("w_ffn_in", after=summed)

    for n in SMALL:
        lo, hi = SMALL_ROWS[n]
        shape = weights[n].shape
        g2 = summed[lo:hi].reshape(1, -1)[:, :weights[n].size]
        d2, m2, v2 = _adamw(weights[n].reshape(g2.shape), g2, m_in[n].reshape(g2.shape), v_in[n].reshape(g2.shape),
                            name=f"adamw_{n}")
        out_g[n], out_d[n], out_m[n], out_v[n] = [t.reshape(shape) for t in (g2, d2, m2, v2)]
    return (loss, grad_x[None], *[out_g[n] for n in ORDER], *[out_d[n] for n in ORDER],
            *[out_m[n] for n in ORDER], *[out_v[n] for n in ORDER])
```

```python
import numpy as np
import jax
import jax.numpy as jnp
from jax import lax
from jax.experimental import pallas as pl
from jax.experimental.pallas import tpu as pltpu

F32 = jnp.float32
_CD = jnp.bfloat16

D_MODEL = 1024
HEAD_DIM = 64
DIL_PAIRS = ((128, 1), (512, 4), (2048, 16))
N_DIL_GROUPS = 3
DIL_HEADS = 4
DIL_W = 128
DIL_OUT = DIL_HEADS * HEAD_DIM
DIL_WIDTH = N_DIL_GROUPS * DIL_OUT
N_FOX_HEADS = 8
FOX_WIDTH = N_FOX_HEADS * HEAD_DIM
D_FF = 2816
QKV_COLS = 3 * DIL_WIDTH + 3 * FOX_WIDTH
F_PAD = 128
RMS_EPS = 1e-6
NEG_INF = -1e30
ATTN_SCALE = HEAD_DIM ** -0.5
ADAM_LR, ADAM_B1, ADAM_B2, ADAM_EPS, ADAM_WD, ADAM_STEP = 0.001, 0.9, 0.999, 1e-08, 0.01, 10

VMEM_LIMIT = 48 * 1024 * 1024
VMEM_LIMIT_RESIDENT = 56 * 1024 * 1024
LANES = 128
MESH = pl.DeviceIdType.MESH
HBM_SPEC = pl.BlockSpec(memory_space=pltpu.HBM)


def _pcall(body, after=None, **kw):
    if after is None:
        return pl.pallas_call(body, **kw)
    n_in = len(kw["in_specs"])
    kw["in_specs"] = list(kw["in_specs"]) + [pl.BlockSpec(memory_space=pl.ANY)]

    def tied(*refs):
        return body(*refs[:n_in], *refs[n_in + 1:])

    call = pl.pallas_call(tied, **kw)
    return lambda *args: call(*args, after)


def _params(*sem):
    return pltpu.CompilerParams(dimension_semantics=sem, vmem_limit_bytes=VMEM_LIMIT)


def _pick(dim, pref):
    t = (min(pref, dim) // 128) * 128
    while t >= 128:
        if dim % t == 0:
            return t
        t -= 128
    return dim


def _rms_bwd_store(r, x_ref, g_ref, dres_ref, o_ref, dg_ref):
    xv = x_ref[...]
    rs = lax.rsqrt(jnp.mean(xv * xv, axis=-1, keepdims=True) + RMS_EPS)
    xh = xv * rs
    dxh = r * g_ref[...]
    o_ref[...] = dres_ref[...] + rs * (dxh - xh * jnp.mean(dxh * xh, axis=-1, keepdims=True))
    part = jnp.sum(r * xh, axis=0, keepdims=True)
    first = pl.program_id(0) == 0

    @pl.when(first)
    def _():
        dg_ref[...] = part

    @pl.when(jnp.logical_not(first))
    def _():
        dg_ref[...] += part


def _d_h2(dgu, w_blocks, x, g, dres, *, name, tm=256, after=None):
    _, S, F = dgu.shape
    n, D, C = w_blocks.shape
    assert n == 4 and F == 2 * C
    nt = (((1,), (1,)), ((), ()))

    def body(dg_ref, du_ref, w_ref, x_ref, g_ref, dres_ref, o_ref, dgn_ref):
        r = None
        for k in range(n):
            a_ref = dg_ref if k < 2 else du_ref
            p = lax.dot_general(a_ref[:, (k % 2) * C:(k % 2 + 1) * C].astype(_CD), w_ref[k].astype(_CD), nt,
                                preferred_element_type=F32)
            r = p if r is None else r + p
        _rms_bwd_store(r, x_ref, g_ref, dres_ref, o_ref, dgn_ref)

    row = pl.BlockSpec((tm, D), lambda i: (i, 0))
    vec = pl.BlockSpec((1, D), lambda i: (0, 0))
    return _pcall(
        body, after, name=name, grid=(S // tm,),
        in_specs=[pl.BlockSpec((None, tm, F), lambda i: (0, i, 0)), pl.BlockSpec((None, tm, F), lambda i: (1, i, 0)),
                  pl.BlockSpec((n, D, C), lambda i: (0, 0, 0)), row, vec, row],
        out_specs=[row, vec], out_shape=[jax.ShapeDtypeStruct((S, D), F32), jax.ShapeDtypeStruct((1, D), F32)],
        compiler_params=_params("arbitrary"))(dgu, dgu, w_blocks, x, g, dres)


def _mm(a, b, *, name, ta=False, tb=False, out_dtype=F32, add=None, tm=1024, tn=512, tk=2048, after=None,
        b_blocks=False, out_blocks=None, a_halves=False, b_halves=False, rms_bwd=None, more=()):
    if a_halves:
        M, K = a.shape[1], 2 * a.shape[2]
    elif ta:
        K, M = a.shape
    else:
        M, K = a.shape
    if b_halves:
        b_rows, b_cols = b.shape[1], 2 * b.shape[2]
    else:
        b_rows, b_cols = (b.shape[1], b.shape[0] * b.shape[2]) if b_blocks else b.shape
    if tb:
        N, K2 = b_rows, b_cols
    else:
        K2, N = b_rows, b_cols
    assert K == K2, (a.shape, b.shape)
    shard = b.shape[2] if b_blocks else None
    tm = _pick(M, tm)
    tn = _pick(shard if (b_blocks and not tb) else (out_blocks or N), tn)
    tk = _pick(shard if (b_blocks and tb) else K, tk)
    nk = K // tk
    dn = (((0 if ta else 1,), (1 if tb else 0,)), ((), ()))
    has_add = add is not None
    assert not (has_add and out_blocks)
    has_norm = rms_bwd is not None
    if has_norm:
        tn = N
        assert not out_blocks and out_dtype == F32
    assert not more or (nk == 1 and tb and not ta)

    def body(*refs):
        a_ref, b_ref = refs[0], refs[1]
        rest = list(refs[2:])
        more_refs = [(rest.pop(0), rest.pop(0)) for _ in more]
        add_ref = rest.pop(0) if has_add else None
        x_ref, g_ref, dres_ref = (rest.pop(0), rest.pop(0), rest.pop(0)) if has_norm else (None, None, None)
        o_ref = rest.pop(0)
        dg_ref = rest.pop(0) if has_norm else None
        bv = b_ref[0] if b_blocks else b_ref[...]
        p = lax.dot_general(a_ref[...].astype(_CD), bv.astype(_CD), dn, preferred_element_type=F32)
        for a2_ref, b2_ref in more_refs:
            p += lax.dot_general(a2_ref[...].astype(_CD), b2_ref[...].astype(_CD), dn, preferred_element_type=F32)

        def finish(r):
            if has_add:
                r = r + add_ref[...]
            if has_norm:
                _rms_bwd_store(r, x_ref, g_ref, dres_ref, o_ref, dg_ref)
            elif out_blocks:
                o_ref[0] = r.astype(out_dtype)
            else:
                o_ref[...] = r.astype(out_dtype)

        if nk == 1:
            finish(p)
        else:
            acc_ref = rest.pop(0)
            k = pl.program_id(2)

            @pl.when(k == 0)
            def _():
                acc_ref[...] = p

            @pl.when(k > 0)
            def _():
                acc_ref[...] += p

            @pl.when(k == nk - 1)
            def _():
                finish(acc_ref[...])

    if a_halves:
        ka = (K // 2) // tk
        a_spec = pl.BlockSpec((None, tm, tk), lambda i, j, k: (k // ka, i, k % ka))
    else:
        a_spec = pl.BlockSpec((tk, tm), lambda i, j, k: (k, i)) if ta else pl.BlockSpec((tm, tk), lambda i, j, k: (i, k))
    if b_halves:
        nb_ = (N // 2) // tn
        b_spec = pl.BlockSpec((None, tk, tn), lambda i, j, k: (j // nb_, k, j % nb_))
    elif b_blocks and tb:
        per = shard // tk
        b_spec = pl.BlockSpec((1, tn, tk), lambda i, j, k: (k // per, j, k % per))
    elif b_blocks:
        per = shard // tn
        b_spec = pl.BlockSpec((1, tk, tn), lambda i, j, k: (j // per, k, j % per))
    else:
        b_spec = pl.BlockSpec((tn, tk), lambda i, j, k: (j, k)) if tb else pl.BlockSpec((tk, tn), lambda i, j, k: (k, j))
    if out_blocks:
        oper = out_blocks // tn
        o_spec = pl.BlockSpec((1, tm, tn), lambda i, j, k: (j // oper, i, j % oper))
        out_shape = jax.ShapeDtypeStruct((N // out_blocks, M, out_blocks), out_dtype)
    else:
        o_spec = pl.BlockSpec((tm, tn), lambda i, j, k: (i, j))
        out_shape = jax.ShapeDtypeStruct((M, N), out_dtype)
    in_specs, args = [a_spec, b_spec], (a, b)
    for a2, b2 in more:
        assert a2.shape[0] == M and b2.shape == (N, a2.shape[1]), (a2.shape, b2.shape)
        in_specs += [pl.BlockSpec((tm, a2.shape[1]), lambda i, j, k: (i, 0)),
                     pl.BlockSpec((tn, a2.shape[1]), lambda i, j, k: (j, 0))]
        args += (a2, b2)
    if has_add:
        in_specs, args = in_specs + [o_spec], args + (add,)
    out_specs, semantics = o_spec, ("parallel", "parallel", "arbitrary")
    if has_norm:
        vec = pl.BlockSpec((1, N), lambda i, j, k: (0, 0))
        in_specs += [o_spec, vec, o_spec]
        args += tuple(rms_bwd)
        out_specs, out_shape = [o_spec, vec], [out_shape, jax.ShapeDtypeStruct((1, N), F32)]
        semantics = ("arbitrary", "arbitrary", "arbitrary")
    return _pcall(
        body, after, name=name, grid=(M // tm, N // tn, nk), in_specs=in_specs, out_specs=out_specs,
        out_shape=out_shape,
        scratch_shapes=[pltpu.VMEM((tm, tn), F32)] if nk > 1 else [],
        compiler_params=_params(*semantics),
    )(*args)


def _rms_fwd(x, g, *, name, tm=512, after=None):
    S, D = x.shape

    def body(x_ref, g_ref, h_ref):
        xv = x_ref[...]
        r = lax.rsqrt(jnp.mean(xv * xv, axis=-1, keepdims=True) + RMS_EPS)
        h_ref[...] = ((xv * r) * g_ref[...]).astype(h_ref.dtype)

    row = pl.BlockSpec((tm, D), lambda i: (i, 0))
    return _pcall(body, after, name=name, grid=(S // tm,), in_specs=[row, pl.BlockSpec((1, D), lambda i: (0, 0))],
                  out_specs=row, out_shape=jax.ShapeDtypeStruct((S, D), _CD), compiler_params=_params("parallel"))(x, g)


def _ffn_down_loss(act, w_down, x1, g, tgt, *, name, tm=512):
    S, D = x1.shape
    F = act.shape[1]

    def body(a_ref, b_ref, x_ref, g_ref, t_ref, loss_ref, dx_ref, dg_ref):
        xv = x_ref[...] + jnp.dot(a_ref[...].astype(_CD), b_ref[...].astype(_CD), preferred_element_type=F32)
        gv = g_ref[...]
        r = lax.rsqrt(jnp.mean(xv * xv, axis=-1, keepdims=True) + RMS_EPS)
        xh = xv * r
        err = xh * gv - t_ref[...]
        lpart = 0.5 * jnp.sum(jnp.mean(err * err, axis=-1, keepdims=True), axis=0, keepdims=True)
        dy = err * (1.0 / D)
        dxh = dy * gv
        dx_ref[...] = r * (dxh - xh * jnp.mean(dxh * xh, axis=-1, keepdims=True))
        gpart = jnp.sum(dy * xh, axis=0, keepdims=True)

        @pl.when(pl.program_id(0) == 0)
        def _():
            loss_ref[...] = lpart
            dg_ref[...] = gpart

        @pl.when(pl.program_id(0) > 0)
        def _():
            loss_ref[...] += lpart
            dg_ref[...] += gpart

    row = pl.BlockSpec((tm, D), lambda i: (i, 0))
    vec = pl.BlockSpec((1, D), lambda i: (0, 0))
    one = pl.BlockSpec((1, 1), lambda i: (0, 0))
    return _pcall(body, name=name, grid=(S // tm,),
                  in_specs=[pl.BlockSpec((tm, F), lambda i: (i, 0)), pl.BlockSpec((F, D), lambda i: (0, 0)), row, vec, row],
                  out_specs=[one, row, vec],
                  out_shape=[jax.ShapeDtypeStruct((1, 1), F32), jax.ShapeDtypeStruct((S, D), F32),
                             jax.ShapeDtypeStruct((1, D), F32)],
                  compiler_params=_params("arbitrary"))(act, w_down, x1, g, tgt)


def _sigmoid(z):
    return 1.0 / (1.0 + jnp.exp(-z))


def _gated_mix_out(gl, bg, ya, yb, w_out, x, g, *, name, tm=512):
    S, D = ya.shape

    def body(za_ref, zb_ref, ba_ref, bb_ref, ya_ref, yb_ref, w_ref, x_ref, g_ref, m_ref, x1_ref, h_ref):
        ga = _sigmoid(za_ref[...].astype(F32) + ba_ref[...])
        gb = _sigmoid(zb_ref[...].astype(F32) + bb_ref[...])
        merged = (ga * ya_ref[...].astype(F32) + gb * yb_ref[...].astype(F32)).astype(m_ref.dtype)
        m_ref[...] = merged
        x1 = x_ref[...] + jnp.dot(merged, w_ref[...].astype(_CD), preferred_element_type=F32)
        x1_ref[...] = x1
        rs = lax.rsqrt(jnp.mean(x1 * x1, axis=-1, keepdims=True) + RMS_EPS)
        h_ref[...] = ((x1 * rs) * g_ref[...]).astype(h_ref.dtype)

    lo = pl.BlockSpec((tm, D), lambda i: (i, 0))
    hi = pl.BlockSpec((tm, D), lambda i: (i, 1))
    vlo = pl.BlockSpec((1, D), lambda i: (0, 0))
    vhi = pl.BlockSpec((1, D), lambda i: (0, 1))
    whole = pl.BlockSpec((D, D), lambda i: (0, 0))
    return _pcall(body, name=name, grid=(S // tm,), in_specs=[lo, hi, vlo, vhi, lo, lo, whole, lo, vlo],
                  out_specs=[lo, lo, lo],
                  out_shape=[jax.ShapeDtypeStruct((S, D), _CD), jax.ShapeDtypeStruct((S, D), F32),
                             jax.ShapeDtypeStruct((S, D), _CD)],
                  compiler_params=_params("parallel"))(gl, gl, bg, bg, ya, yb, w_out, x, g)


def _gate_bwd(dx1, w_out, gl, bg, ya, yb, *, name, tm=512):
    S, D = ya.shape
    nt = (((1,), (1,)), ((), ()))

    def body(dx_ref, w_ref, za_ref, zb_ref, ba_ref, bb_ref, ya_ref, yb_ref, dya_ref, dyb_ref, dgl_ref, dbg_ref):
        dmv = lax.dot_general(dx_ref[...].astype(_CD), w_ref[...].astype(_CD), nt, preferred_element_type=F32)
        ga = _sigmoid(za_ref[...].astype(F32) + ba_ref[...])
        gb = _sigmoid(zb_ref[...].astype(F32) + bb_ref[...])
        dya_ref[...] = (dmv * ga).astype(dya_ref.dtype)
        dyb_ref[...] = (dmv * gb).astype(dyb_ref.dtype)
        dza = dmv * ya_ref[...].astype(F32) * ga * (1.0 - ga)
        dzb = dmv * yb_ref[...].astype(F32) * gb * (1.0 - gb)
        dgl_ref[:, :D] = dza.astype(dgl_ref.dtype)
        dgl_ref[:, D:] = dzb.astype(dgl_ref.dtype)
        pa = jnp.sum(dza, axis=0, keepdims=True)
        pb = jnp.sum(dzb, axis=0, keepdims=True)

        @pl.when(pl.program_id(0) == 0)
        def _():
            dbg_ref[:, :D] = pa
            dbg_ref[:, D:] = pb

        @pl.when(pl.program_id(0) > 0)
        def _():
            dbg_ref[:, :D] += pa
            dbg_ref[:, D:] += pb

    lo = pl.BlockSpec((tm, D), lambda i: (i, 0))
    hi = pl.BlockSpec((tm, D), lambda i: (i, 1))
    vlo = pl.BlockSpec((1, D), lambda i: (0, 0))
    vhi = pl.BlockSpec((1, D), lambda i: (0, 1))
    wide = pl.BlockSpec((tm, 2 * D), lambda i: (i, 0))
    vwide = pl.BlockSpec((1, 2 * D), lambda i: (0, 0))
    whole = pl.BlockSpec((D, D), lambda i: (0, 0))
    return _pcall(body, name=name, grid=(S // tm,), in_specs=[lo, whole, lo, hi, vlo, vhi, lo, lo],
                  out_specs=[lo, lo, wide, vwide],
                  out_shape=[jax.ShapeDtypeStruct((S, D), _CD), jax.ShapeDtypeStruct((S, D), _CD),
                             jax.ShapeDtypeStruct((S, 2 * D), _CD), jax.ShapeDtypeStruct((1, 2 * D), F32)],
                  compiler_params=_params("arbitrary"))(dx1, w_out, gl, gl, bg, bg, ya, yb)


def _ffn_in_act(h2, w_blocks, *, name, tm=512):
    S, D = h2.shape
    _, _, C = w_blocks.shape

    def body(a_ref, bg_ref, bu_ref, g_ref, u_ref, o_ref):
        av = a_ref[...].astype(_CD)
        gv = jnp.dot(av, bg_ref[0].astype(_CD), preferred_element_type=F32)
        uv = jnp.dot(av, bu_ref[0].astype(_CD), preferred_element_type=F32)
        g_ref[...] = gv.astype(g_ref.dtype)
        u_ref[...] = uv.astype(u_ref.dtype)
        o_ref[...] = (gv * _sigmoid(gv) * uv).astype(o_ref.dtype)

    out = pl.BlockSpec((tm, C), lambda i, j: (i, j))
    shp = jax.ShapeDtypeStruct((S, 2 * C), _CD)
    return _pcall(body, name=name, grid=(S // tm, 2),
                  in_specs=[pl.BlockSpec((tm, D), lambda i, j: (i, 0)), pl.BlockSpec((1, D, C), lambda i, j: (j, 0, 0)),
                            pl.BlockSpec((1, D, C), lambda i, j: (2 + j, 0, 0))],
                  out_specs=[out, out, out], out_shape=[shp, shp, shp],
                  compiler_params=_params("parallel", "arbitrary"))(h2, w_blocks, w_blocks)


def _d_swiglu(dx, w_down, gate, up, *, name, tm=512, tn=1408):
    S, D = dx.shape
    F = w_down.shape[0]
    nt = (((1,), (1,)), ((), ()))

    def body(a_ref, b_ref, g_ref, u_ref, o_ref):
        dv = lax.dot_general(a_ref[...].astype(_CD), b_ref[...].astype(_CD), nt, preferred_element_type=F32)
        gv = g_ref[...].astype(F32)
        sg = _sigmoid(gv)
        o_ref[0] = (dv * u_ref[...].astype(F32) * (sg * (1.0 + gv * (1.0 - sg)))).astype(o_ref.dtype)
        o_ref[1] = (dv * (gv * sg)).astype(o_ref.dtype)

    tile = pl.BlockSpec((tm, tn), lambda i, j: (i, j))
    return _pcall(body, name=name, grid=(S // tm, F // tn),
                  in_specs=[pl.BlockSpec((tm, D), lambda i, j: (i, 0)), pl.BlockSpec((tn, D), lambda i, j: (j, 0)),
                            tile, tile],
                  out_specs=pl.BlockSpec((2, tm, tn), lambda i, j: (0, i, j)),
                  out_shape=jax.ShapeDtypeStruct((2, S, F), _CD),
                  compiler_params=_params("parallel", "arbitrary"))(dx, w_down, gate, up)


def _split3(x):
    hi = x.astype(jnp.bfloat16)
    r1 = x - hi.astype(F32)
    mid = r1.astype(jnp.bfloat16)
    lo = (r1 - mid.astype(F32)).astype(jnp.bfloat16)
    return hi, mid, lo


def _ones_dot_left(ones, x):
    return sum(jnp.dot(ones, p, preferred_element_type=F32) for p in _split3(x))


def _ones_dot_right(x, ones):
    return sum(jnp.dot(p, ones, preferred_element_type=F32) for p in _split3(x))


def _head_sum(x):
    n = x.shape[1]
    r = lax.broadcasted_iota(jnp.int32, (n, n), 0) // HEAD_DIM
    c = lax.broadcasted_iota(jnp.int32, (n, n), 1) // HEAD_DIM
    return _ones_dot_right(x, (r == c).astype(jnp.bfloat16))


def _log_sigmoid(z):
    e = jnp.exp(-jnp.abs(z))
    t = 1.0 + e
    log1p_e = jnp.where(t == 1.0, e, jnp.log(t) * (e / jnp.where(t == 1.0, 1.0, t - 1.0)))
    return jnp.minimum(z, 0.0) - log1p_e


def _fox_cumsum(zf, bf, *, name):
    S, W = zf.shape
    nb = S // 128

    def body(z_ref, b_ref, c_ref):
        tri = (lax.broadcasted_iota(jnp.int32, (128, 128), 0) >= lax.broadcasted_iota(jnp.int32, (128, 128), 1))
        tri = tri.astype(jnp.bfloat16)

        def step(i, carry):
            rows = pl.ds(pl.multiple_of(i * 128, 128), 128)
            lf = _log_sigmoid(z_ref[rows, :] + b_ref[...])
            cb = _ones_dot_left(tri, lf) + carry
            c_ref[rows, :] = cb
            return cb[127:128, :]

        lax.fori_loop(0, nb, step, jnp.zeros((1, W), F32))

    return _pcall(body, name=name, out_shape=jax.ShapeDtypeStruct((S, W), F32),
                  compiler_params=pltpu.CompilerParams(vmem_limit_bytes=VMEM_LIMIT))(zf, bf)


def _fox_cumsum_bwd(dc, zf, bf, *, name):
    S, W = zf.shape
    nb = S // 128

    def body(dc_ref, z_ref, b_ref, dz_ref, db_ref):
        tri = (lax.broadcasted_iota(jnp.int32, (128, 128), 0) <= lax.broadcasted_iota(jnp.int32, (128, 128), 1))
        tri = tri.astype(jnp.bfloat16)

        def step(k, carry):
            tail, acc = carry
            i = nb - 1 - k
            rows = pl.ds(pl.multiple_of(i * 128, 128), 128)
            dlf = _ones_dot_left(tri, dc_ref[rows, :]) + tail
            dz = dlf * _sigmoid(-(z_ref[rows, :] + b_ref[...]))
            dz_ref[rows, :] = dz
            return dlf[0:1, :], acc + jnp.sum(dz, axis=0, keepdims=True)

        _, acc = lax.fori_loop(0, nb, step, (jnp.zeros((1, W), F32), jnp.zeros((1, W), F32)))
        db_ref[...] = acc

    return _pcall(body, name=name,
                  out_shape=[jax.ShapeDtypeStruct((S, W), F32), jax.ShapeDtypeStruct((1, W), F32)],
                  compiler_params=pltpu.CompilerParams(vmem_limit_bytes=VMEM_LIMIT))(dc, zf, bf)


def _proj_dil(h, w_qkv, *, name, tm=1024):
    S, D = h.shape
    tn = DIL_WIDTH

    def body(a_ref, b_ref, *rest):
        outs, acc = rest[:N_DIL_GROUPS], rest[N_DIL_GROUPS]
        prod = jnp.dot(a_ref[...].astype(_CD), b_ref[...].astype(_CD), preferred_element_type=F32)
        for k in range(tn // LANES):
            acc[k] = prod[:, k * LANES:(k + 1) * LANES]
        for g, (_, d) in enumerate(DIL_PAIRS):
            for half in range(DIL_OUT // LANES):
                k = g * (DIL_OUT // LANES) + half
                cols = slice(half * LANES, (half + 1) * LANES)
                for r in range(d):
                    rows = pl.ds(r, tm // d, stride=d) if d > 1 else slice(None)
                    outs[g][0, r, :, cols] = acc[k, rows, :].astype(outs[g].dtype)

    out_specs = [pl.BlockSpec((1, d, tm // d, DIL_OUT), lambda i, j: (j, 0, i, 0)) for _, d in DIL_PAIRS]
    out_shape = [jax.ShapeDtypeStruct((3, d, S // d, DIL_OUT), _CD) for _, d in DIL_PAIRS]
    outs = _pcall(body, name=name, grid=(S // tm, 3),
                  in_specs=[pl.BlockSpec((tm, D), lambda i, j: (i, 0)), pl.BlockSpec((D, tn), lambda i, j: (0, j))],
                  out_specs=out_specs, out_shape=out_shape, scratch_shapes=[pltpu.VMEM((tn // LANES, tm, LANES), F32)],
                  compiler_params=_params("parallel", "arbitrary"))(h, w_qkv)
    return [o.reshape(3, S, DIL_OUT) for o in outs]


def _dil_start(block, S, dilation):
    sub = S // dilation
    u0 = block * DIL_W
    return (u0 % sub) * dilation + u0 // sub


def _dil_slopes(group):
    h = np.arange(1, N_DIL_GROUPS * DIL_HEADS + 1, dtype=np.float32)
    s = (np.float32(2.0) ** (np.float32(-8.0) * h / np.float32(N_DIL_GROUPS * DIL_HEADS))).astype(np.float32)
    return [float(v) for v in s.reshape(N_DIL_GROUPS, DIL_HEADS)[group]]


def _dil_tiles(i, n, blocks_per_seq):
    qi = lax.broadcasted_iota(jnp.int32, (DIL_W, 2 * DIL_W), 0)
    kj = lax.broadcasted_iota(jnp.int32, (DIL_W, 2 * DIL_W), 1)
    rel = qi + DIL_W - kj
    first = ((4 * n + i) % blocks_per_seq) == 0
    valid = jnp.logical_and(jnp.logical_and(rel >= 0, rel <= DIL_W), jnp.logical_or(kj >= DIL_W, jnp.logical_not(first)))
    return valid, rel.astype(F32)


def _dil_window(cur_ref, prev_ref, i, cols):
    if i > 0:
        return cur_ref[(i - 1) * DIL_W:(i + 1) * DIL_W, cols]
    return jnp.concatenate([prev_ref[:, cols], cur_ref[:DIL_W, cols]], axis=0)


CHUNK = 4 * DIL_W


def _dil_rows(block, S, dilation):
    start = _dil_start(block, S, dilation)
    return pl.ds(start, DIL_W, stride=dilation) if dilation > 1 else pl.ds(start, DIL_W)


def SPLIT(S):
    return (DIL_OUT // LANES, S, LANES)


def _dil_fwd(qkv, group, *, name):
    S = qkv.shape[1]
    dilation = DIL_PAIRS[group][1]
    bps = (S // dilation) // DIL_W
    slopes = _dil_slopes(group)
    nt = (((1,), (1,)), ((), ()))

    def body(q_ref, k_ref, v_ref, kp_ref, vp_ref, on_ref, ln_ref, o_ref, l_ref):
        n = pl.program_id(0)
        for i in range(4):
            valid, rel = _dil_tiles(i, n, bps)
            rows = slice(i * DIL_W, (i + 1) * DIL_W)
            for h in range(DIL_HEADS):
                cols = slice(h * HEAD_DIM, (h + 1) * HEAD_DIM)
                qh = q_ref[rows, cols]
                k2, v2 = _dil_window(k_ref, kp_ref, i, cols), _dil_window(v_ref, vp_ref, i, cols)
                s = lax.dot_general(qh, k2, nt, preferred_element_type=F32) * ATTN_SCALE - (slopes[h] * dilation) * rel
                s = jnp.where(valid, s, NEG_INF)
                m = jnp.max(s, axis=-1, keepdims=True)
                p = jnp.exp(s - m)
                den = jnp.sum(p, axis=-1, keepdims=True)
                acc = jnp.dot(p.astype(_CD), v2, preferred_element_type=F32)
                o_ref[rows, cols] = acc / den
                l_ref[rows, cols] = jnp.broadcast_to(m + jnp.log(den), (DIL_W, HEAD_DIM))
        for i in range(4):
            rows = slice(i * DIL_W, (i + 1) * DIL_W)
            nat = _dil_rows(4 * n + i, S, dilation)
            for half in range(DIL_OUT // LANES):
                cols = slice(half * LANES, (half + 1) * LANES)
                on_ref[half, nat, :] = o_ref[rows, cols]
                ln_ref[half, nat, :] = l_ref[rows, cols]

    def cur(which):
        return pl.BlockSpec((None, CHUNK, DIL_OUT), lambda n: (which, n, 0))

    def prev(which):
        return pl.BlockSpec((None, DIL_W, DIL_OUT), lambda n: (which, jnp.maximum(4 * n - 1, 0), 0))

    whole = pl.BlockSpec(SPLIT(S), lambda n: (0, 0, 0))
    return _pcall(body, name=name, grid=(S // CHUNK,), in_specs=[cur(0), cur(1), cur(2), prev(1), prev(2)],
                  out_specs=[whole, whole],
                  out_shape=[jax.ShapeDtypeStruct(SPLIT(S), F32), jax.ShapeDtypeStruct(SPLIT(S), F32)],
                  scratch_shapes=[pltpu.VMEM((CHUNK, DIL_OUT), F32), pltpu.VMEM((CHUNK, DIL_OUT), F32)],
                  compiler_params=_params("arbitrary"))(qkv, qkv, qkv, qkv, qkv)


STAT_OFFSET = HEAD_DIM // 2


def _dil_bwd(qkv, stats, do, group, *, name):
    S = qkv.shape[1]
    dilation = DIL_PAIRS[group][1]
    bps = (S // dilation) // DIL_W
    slopes = _dil_slopes(group)
    nchunk = S // CHUNK
    nt = (((1,), (1,)), ((), ()))
    tn = (((0,), (0,)), ((), ()))

    def body(q_ref, k_ref, v_ref, kp_ref, vp_ref, ln_ref, don_ref, dqn_ref, dkn_ref, dvn_ref,
             dk_s, dv_s, l_ref, do_ref, dq_ref):
        step = pl.program_id(0)
        n = nchunk - 1 - step
        for i in range(4):
            rows = slice(i * DIL_W, (i + 1) * DIL_W)
            nat = _dil_rows(4 * n + i, S, dilation)
            for half in range(DIL_OUT // LANES):
                cols = slice(half * LANES, (half + 1) * LANES)
                l_ref[rows, cols] = ln_ref[half, nat, :]
                do_ref[rows, cols] = don_ref[half, nat, :]

        @pl.when(step == 0)
        def _():
            dk_s[:, CHUNK:] = jnp.zeros((DIL_OUT, DIL_W), F32)
            dv_s[:, CHUNK:] = jnp.zeros((DIL_OUT, DIL_W), F32)

        dk_s[:, :CHUNK] = jnp.zeros((DIL_OUT, CHUNK), F32)
        dv_s[:, :CHUNK] = jnp.zeros((DIL_OUT, CHUNK), F32)
        for i in range(4):
            valid, rel = _dil_tiles(i, n, bps)
            rows = slice(i * DIL_W, (i + 1) * DIL_W)
            window = slice(i * DIL_W, (i + 2) * DIL_W)
            for h in range(DIL_HEADS):
                cols = slice(h * HEAD_DIM, (h + 1) * HEAD_DIM)
                qh = q_ref[rows, cols]
                k2, v2 = _dil_window(k_ref, kp_ref, i, cols), _dil_window(v_ref, vp_ref, i, cols)
                lh = l_ref[rows, h * HEAD_DIM:h * HEAD_DIM + 1]
                shift = l_ref[rows, h * HEAD_DIM + STAT_OFFSET:h * HEAD_DIM + STAT_OFFSET + 1]
                s = lax.dot_general(qh, k2, nt, preferred_element_type=F32) * ATTN_SCALE - (slopes[h] * dilation) * rel
                p = jnp.exp(jnp.where(valid, s, NEG_INF) - lh)
                dob = do_ref[rows, cols].astype(_CD)
                ds = p * (lax.dot_general(dob, v2, nt, preferred_element_type=F32) + shift)
                dsb = (ds * ATTN_SCALE).astype(_CD)
                dq_ref[rows, cols] = jnp.dot(dsb, k2, preferred_element_type=F32)
                dk_s[cols, window] += lax.dot_general(qh, dsb, tn, preferred_element_type=F32)
                dv_s[cols, window] += lax.dot_general(dob, p.astype(_CD), tn, preferred_element_type=F32)
        for i in range(4):
            rows = slice(i * DIL_W, (i + 1) * DIL_W)
            done = slice((i + 1) * DIL_W, (i + 2) * DIL_W)
            nat = _dil_rows(4 * n + i, S, dilation)
            dkb, dvb = dk_s[:, done].T, dv_s[:, done].T
            for half in range(DIL_OUT // LANES):
                cols = slice(half * LANES, (half + 1) * LANES)
                dqn_ref[half, nat, :] = dq_ref[rows, cols]
                dkn_ref[half, nat, :] = dkb[:, cols]
                dvn_ref[half, nat, :] = dvb[:, cols]
        dk_s[:, CHUNK:] = dk_s[:, :DIL_W]
        dv_s[:, CHUNK:] = dv_s[:, :DIL_W]

    def cur(which):
        return pl.BlockSpec((None, CHUNK, DIL_OUT), lambda s: (which, nchunk - 1 - s, 0))

    def prev(which):
        return pl.BlockSpec((None, DIL_W, DIL_OUT), lambda s: (which, jnp.maximum(4 * (nchunk - 1 - s) - 1, 0), 0))

    whole = pl.BlockSpec(SPLIT(S), lambda s: (0, 0, 0))
    shp = jax.ShapeDtypeStruct(SPLIT(S), F32)
    tile = pltpu.VMEM((CHUNK, DIL_OUT), F32)
    return _pcall(body, name=name, grid=(nchunk,),
                  in_specs=[cur(0), cur(1), cur(2), prev(1), prev(2), whole, whole],
                  out_specs=[whole, whole, whole], out_shape=[shp, shp, shp],
                  scratch_shapes=[pltpu.VMEM((DIL_OUT, CHUNK + DIL_W), F32), pltpu.VMEM((DIL_OUT, CHUNK + DIL_W), F32),
                                  tile, tile, tile],
                  compiler_params=pltpu.CompilerParams(dimension_semantics=("arbitrary",),
                                                       vmem_limit_bytes=VMEM_LIMIT_RESIDENT))(
        qkv, qkv, qkv, qkv, qkv, stats, do)


def _dil_mix_fwd(os_, ls_, *, name, tm=512):
    nh, S, _ = os_[0].shape

    def body(o0, o1, o2, l0, l1, l2, out_ref):
        for half in range(nh):
            ls = [l0[half], l1[half], l2[half]]
            m = jnp.maximum(jnp.maximum(ls[0], ls[1]), ls[2])
            es = [jnp.exp(l - m) for l in ls]
            den = es[0] + es[1] + es[2]
            mixed = (es[0] * o0[half] + es[1] * o1[half] + es[2] * o2[half]) / den
            out_ref[:, half * LANES:(half + 1) * LANES] = mixed.astype(out_ref.dtype)

    halves = pl.BlockSpec((nh, tm, LANES), lambda i: (0, i, 0))
    row = pl.BlockSpec((tm, nh * LANES), lambda i: (i, 0))
    return _pcall(body, name=name, grid=(S // tm,), in_specs=[halves] * 6, out_specs=row,
                  out_shape=jax.ShapeDtypeStruct((S, nh * LANES), _CD), compiler_params=_params("parallel"))(*os_, *ls_)


def _dil_mix_bwd(doa, os_, ls_, *, name, tm=512, after=None):
    nh, S, _ = os_[0].shape

    def body(d_ref, o0, o1, o2, l0, l1, l2, do0, do1, do2, st0, st1, st2):
        first = lax.broadcasted_iota(jnp.int32, (tm, LANES), 1) % HEAD_DIM < STAT_OFFSET
        for half in range(nh):
            dv = d_ref[:, half * LANES:(half + 1) * LANES]
            ls = [l0[half], l1[half], l2[half]]
            m = jnp.maximum(jnp.maximum(ls[0], ls[1]), ls[2])
            es = [jnp.exp(l - m) for l in ls]
            den = es[0] + es[1] + es[2]
            al = [e / den for e in es]
            da = [_head_sum(dv * o[half]) for o in (o0, o1, o2)]
            mean = al[0] * da[0] + al[1] * da[1] + al[2] * da[2]
            for a, l, do_ref, st_ref in zip(al, ls, (do0, do1, do2), (st0, st1, st2)):
                do_ref[half] = a * dv
                st_ref[half] = jnp.where(first, l, -a * mean)

    halves = pl.BlockSpec((nh, tm, LANES), lambda i: (0, i, 0))
    row = pl.BlockSpec((tm, nh * LANES), lambda i: (i, 0))
    shp = jax.ShapeDtypeStruct((nh, S, LANES), F32)
    return _pcall(body, after, name=name, grid=(S // tm,), in_specs=[row] + [halves] * 6, out_specs=[halves] * 6,
                  out_shape=[shp] * 6, compiler_params=_params("parallel"))(doa, *os_, *ls_)


FOX_T = 512


PACK = 2 * HEAD_DIM
HEAD_PAIRS = N_FOX_HEADS // 2
FOX_HPS = 8
Q_BLOCK0 = 0
K_BLOCK0 = FOX_WIDTH // PACK
V_BLOCK0 = 2 * FOX_WIDTH // PACK


def _pieces(x):
    hi = x.astype(jnp.bfloat16).astype(F32)
    r = x - hi
    mid = r.astype(jnp.bfloat16).astype(F32)
    lo = (r - mid).astype(jnp.bfloat16).astype(F32)
    return [hi, mid, lo]


def _extras(first, second, rows):
    lane = lax.broadcasted_iota(jnp.int32, (rows, HEAD_DIM), 1)
    out = jnp.zeros((rows, HEAD_DIM), F32)
    for base, triple in ((0, first), (3, second)):
        if all(isinstance(v, float) for v in triple) and len(set(triple)) == 1:
            if triple[0] != 0.0:
                out = jnp.where(jnp.logical_and(lane >= base, lane < base + 3), triple[0], out)
        else:
            for idx, val in enumerate(triple):
                out = jnp.where(lane == base + idx, val, out)
    return out


def _head_column(c, h):
    lane = lax.broadcasted_iota(jnp.int32, c.shape, 1)
    return jnp.sum(jnp.where(lane == h, c, 0.0), axis=1, keepdims=True)


ONES3 = [1.0, 1.0, 1.0]
ZEROS3 = [0.0, 0.0, 0.0]


def _fox_pack_fwd(qkv, c, *, name, tm=1024):
    S = qkv.shape[0]

    def body(q_ref, k_ref, v_ref, c_ref, qo_ref, ko_ref, vo_ref):
        hp = pl.program_id(1)
        cv = c_ref[...]
        v_extras = jnp.where(lax.broadcasted_iota(jnp.int32, (tm, HEAD_DIM), 1) < 3, 1.0, 0.0).astype(vo_ref.dtype)
        for hh in range(2):
            ch = _pieces(_head_column(cv, 2 * hp + hh))
            src = slice(hh * HEAD_DIM, (hh + 1) * HEAD_DIM)
            lo = slice(hh * PACK, hh * PACK + HEAD_DIM)
            hi = slice(hh * PACK + HEAD_DIM, (hh + 1) * PACK)
            qo_ref[:, lo] = (q_ref[:, src].astype(F32) * ATTN_SCALE).astype(qo_ref.dtype)
            qo_ref[:, hi] = _extras(ch, ONES3, tm).astype(qo_ref.dtype)
            ko_ref[:, lo] = k_ref[:, src]
            ko_ref[:, hi] = _extras(ONES3, [-p for p in ch], tm).astype(ko_ref.dtype)
            vo_ref[:, lo] = v_ref[:, src]
            vo_ref[:, hi] = v_extras

    def src(block0):
        return pl.BlockSpec((tm, PACK), lambda i, hp: (i, block0 + hp))

    out = pl.BlockSpec((tm, 2 * PACK), lambda i, hp: (i, hp))
    shp = jax.ShapeDtypeStruct((S, N_FOX_HEADS * PACK), _CD)
    return _pcall(body, name=name, grid=(S // tm, HEAD_PAIRS),
                  in_specs=[src(Q_BLOCK0), src(K_BLOCK0), src(V_BLOCK0), pl.BlockSpec((tm, PACK), lambda i, hp: (i, 0))],
                  out_specs=[out, out, out], out_shape=[shp, shp, shp],
                  compiler_params=_params("parallel", "parallel"))(qkv, qkv, qkv, c)


def _fox_fwd(qp, kp, vp, *, name):
    S = qp.shape[0]
    nt = S // FOX_T
    nt_dims = (((1,), (1,)), ((), ()))
    tn_dims = (((0,), (0,)), ((), ()))

    def body(i_tab, j_tab, q_ref, k_ref, v_ref, o_ref, l_ref, m_s, acc_s):
        t = pl.program_id(1)
        i, j = i_tab[t], j_tab[t]

        @pl.when(j == 0)
        def _():
            m_s[...] = jnp.full((FOX_HPS, 1, FOX_T), NEG_INF, F32)
            acc_s[...] = jnp.zeros((FOX_HPS, PACK, FOX_T), F32)

        def tile(diagonal):
            for hh in range(FOX_HPS):
                cols = slice(hh * PACK, (hh + 1) * PACK)
                st = lax.dot_general(k_ref[:, cols], q_ref[:, cols], nt_dims, preferred_element_type=F32)
                if diagonal:
                    key = lax.broadcasted_iota(jnp.int32, (FOX_T, FOX_T), 0)
                    qry = lax.broadcasted_iota(jnp.int32, (FOX_T, FOX_T), 1)
                    st = jnp.where(key <= qry, st, NEG_INF)
                m_old = m_s[hh]
                m_new = jnp.maximum(m_old, jnp.max(st, axis=0, keepdims=True))
                pt = jnp.exp(st - m_new)
                acc_s[hh] = jnp.exp(m_old - m_new) * acc_s[hh] + lax.dot_general(
                    v_ref[:, cols], pt.astype(_CD), tn_dims, preferred_element_type=F32)
                m_s[hh] = m_new

        @pl.when(j < i)
        def _():
            tile(False)

        @pl.when(j == i)
        def _():
            tile(True)
            for hh in range(FOX_HPS):
                acc = acc_s[hh]
                den = acc[HEAD_DIM:HEAD_DIM + 1, :]
                cols = slice(hh * HEAD_DIM, (hh + 1) * HEAD_DIM)
                o_ref[:, cols] = (acc[:HEAD_DIM, :] / den).T
                l_ref[:, cols] = jnp.broadcast_to(m_s[hh] + jnp.log(den), (HEAD_DIM, FOX_T)).T

    pairs = [(i, j) for i in range(nt) for j in range(i + 1)]
    i_tab = jnp.asarray([p[0] for p in pairs], jnp.int32)
    j_tab = jnp.asarray([p[1] for p in pairs], jnp.int32)
    qs = pl.BlockSpec((FOX_T, FOX_HPS * PACK), lambda hp, t, it, jt: (it[t], hp))
    ks = pl.BlockSpec((FOX_T, FOX_HPS * PACK), lambda hp, t, it, jt: (jt[t], hp))
    os_ = pl.BlockSpec((FOX_T, FOX_HPS * HEAD_DIM), lambda hp, t, it, jt: (it[t], hp))
    shp = jax.ShapeDtypeStruct((S, FOX_WIDTH), F32)
    grid_spec = pltpu.PrefetchScalarGridSpec(
        num_scalar_prefetch=2, grid=(N_FOX_HEADS // FOX_HPS, len(pairs)), in_specs=[qs, ks, ks], out_specs=[os_, os_],
        scratch_shapes=[pltpu.VMEM((FOX_HPS, 1, FOX_T), F32), pltpu.VMEM((FOX_HPS, PACK, FOX_T), F32)])
    return _pcall(body, name=name, grid_spec=grid_spec, out_shape=[shp, shp],
                  compiler_params=_params("parallel", "arbitrary"))(i_tab, j_tab, qp, kp, vp)


def _fox_pack_bwd(qkv, c, o, lse, do, *, name, tm=1024, after=None):
    S = qkv.shape[0]

    def body(q_ref, c_ref, o_ref, l_ref, do_ref, qo_ref, do_out_ref):
        hp = pl.program_id(1)
        cv = c_ref[...]
        for hh in range(2):
            src = slice(hh * HEAD_DIM, (hh + 1) * HEAD_DIM)
            lo = slice(hh * PACK, hh * PACK + HEAD_DIM)
            hi = slice(hh * PACK + HEAD_DIM, (hh + 1) * PACK)
            shift = _head_column(cv, 2 * hp + hh) - l_ref[:, hh * HEAD_DIM:hh * HEAD_DIM + 1]
            dov = do_ref[:, src]
            dsum = jnp.sum(dov * o_ref[:, src], axis=-1, keepdims=True)
            qo_ref[:, lo] = (q_ref[:, src].astype(F32) * ATTN_SCALE).astype(qo_ref.dtype)
            qo_ref[:, hi] = _extras(_pieces(shift), ONES3, tm).astype(qo_ref.dtype)
            do_out_ref[:, lo] = dov.astype(do_out_ref.dtype)
            do_out_ref[:, hi] = _extras(_pieces(-dsum), ZEROS3, tm).astype(do_out_ref.dtype)

    pair = pl.BlockSpec((tm, PACK), lambda i, hp: (i, hp))
    out = pl.BlockSpec((tm, 2 * PACK), lambda i, hp: (i, hp))
    shp = jax.ShapeDtypeStruct((S, N_FOX_HEADS * PACK), _CD)
    return _pcall(body, after, name=name, grid=(S // tm, HEAD_PAIRS),
                  in_specs=[pl.BlockSpec((tm, PACK), lambda i, hp: (i, Q_BLOCK0 + hp)),
                            pl.BlockSpec((tm, PACK), lambda i, hp: (i, 0)), pair, pair, pair],
                  out_specs=[out, out], out_shape=[shp, shp],
                  compiler_params=_params("parallel", "parallel"))(qkv, c, o, lse, do)


def _fox_bwd(qp, kp, vp, dop, *, name):
    S = qp.shape[0]
    nt = S // FOX_T
    nt_dims = (((1,), (1,)), ((), ()))
    tn_dims = (((0,), (0,)), ((), ()))

    def body(i_tab, j_tab, q_ref, k_ref, v_ref, do_ref, dq_ref, dk_ref, dv_ref, dc_ref, dr_ref,
             dq_s, dk_s, dv_s, dc_s, dr_s):
        t = pl.program_id(1)
        i, j = i_tab[t], j_tab[t]

        @pl.when(t == 0)
        def _():
            dq_s[...] = jnp.zeros((S, FOX_HPS * PACK), F32)
            dr_s[...] = jnp.zeros((FOX_HPS, 1, S), F32)

        @pl.when(i == j)
        def _():
            dk_s[...] = jnp.zeros((FOX_T, FOX_HPS * PACK), F32)
            dv_s[...] = jnp.zeros((FOX_T, FOX_HPS * PACK), F32)
            dc_s[...] = jnp.zeros((FOX_HPS, FOX_T, 1), F32)

        def tile(diagonal):
            rows = pl.ds(pl.multiple_of(i * FOX_T, FOX_T), FOX_T)
            for hh in range(FOX_HPS):
                cols = slice(hh * PACK, (hh + 1) * PACK)
                qv, kv, vv, dov = q_ref[:, cols], k_ref[:, cols], v_ref[:, cols], do_ref[:, cols]
                pt = jnp.exp(lax.dot_general(kv, qv, nt_dims, preferred_element_type=F32))
                if diagonal:
                    key = lax.broadcasted_iota(jnp.int32, (FOX_T, FOX_T), 0)
                    qry = lax.broadcasted_iota(jnp.int32, (FOX_T, FOX_T), 1)
                    pt = jnp.where(key <= qry, pt, 0.0)
                dst = pt * lax.dot_general(vv, dov, nt_dims, preferred_element_type=F32)
                dsb = dst.astype(_CD)
                dc_s[hh] += jnp.sum(dst, axis=1, keepdims=True)
                dr_s[hh, :, rows] += jnp.sum(dst, axis=0, keepdims=True)
                dv_s[:, cols] += jnp.dot(pt.astype(_CD), dov, preferred_element_type=F32)
                dk_s[:, cols] += jnp.dot(dsb, qv, preferred_element_type=F32)
                dq_s[rows, cols] += lax.dot_general(dsb, kv, tn_dims, preferred_element_type=F32)

        @pl.when(i > j)
        def _():
            tile(False)

        @pl.when(i == j)
        def _():
            tile(True)

        @pl.when(i == nt - 1)
        def _():
            for hh in range(FOX_HPS):
                src = slice(hh * PACK, hh * PACK + HEAD_DIM)
                dst_cols = slice(hh * HEAD_DIM, (hh + 1) * HEAD_DIM)
                dk_ref[:, dst_cols] = dk_s[:, src].astype(dk_ref.dtype)
                dv_ref[:, dst_cols] = dv_s[:, src].astype(dv_ref.dtype)
                dc_ref[:, dst_cols] = jnp.broadcast_to(dc_s[hh], (FOX_T, HEAD_DIM))

        @pl.when(t == len(pairs) - 1)
        def _():
            for hh in range(FOX_HPS):
                dq_ref[:, hh * HEAD_DIM:(hh + 1) * HEAD_DIM] = (
                    dq_s[:, hh * PACK:hh * PACK + HEAD_DIM] * ATTN_SCALE).astype(dq_ref.dtype)
            dr_ref[...] = dr_s[...]

    pairs = [(i, j) for j in range(nt) for i in range(j, nt)]
    i_tab = jnp.asarray([p[0] for p in pairs], jnp.int32)
    j_tab = jnp.asarray([p[1] for p in pairs], jnp.int32)
    wide, narrow = FOX_HPS * PACK, FOX_HPS * HEAD_DIM
    qs = pl.BlockSpec((FOX_T, wide), lambda hp, t, it, jt: (it[t], hp))
    ks = pl.BlockSpec((FOX_T, wide), lambda hp, t, it, jt: (jt[t], hp))
    whole = pl.BlockSpec((S, narrow), lambda hp, t, it, jt: (0, hp))
    cs = pl.BlockSpec((FOX_T, narrow), lambda hp, t, it, jt: (jt[t], hp))
    rs = pl.BlockSpec((FOX_HPS, 1, S), lambda hp, t, it, jt: (hp, 0, 0))
    shp = jax.ShapeDtypeStruct((S, FOX_WIDTH), _CD)
    grid_spec = pltpu.PrefetchScalarGridSpec(
        num_scalar_prefetch=2, grid=(N_FOX_HEADS // FOX_HPS, len(pairs)), in_specs=[qs, ks, ks, qs],
        out_specs=[whole, cs, cs, cs, rs],
        scratch_shapes=[pltpu.VMEM((S, wide), F32), pltpu.VMEM((FOX_T, wide), F32),
                        pltpu.VMEM((FOX_T, wide), F32), pltpu.VMEM((FOX_HPS, FOX_T, 1), F32),
                        pltpu.VMEM((FOX_HPS, 1, S), F32)])
    return _pcall(body, name=name, grid_spec=grid_spec,
                  out_shape=[shp, shp, shp, jax.ShapeDtypeStruct((S, FOX_WIDTH), F32),
                             jax.ShapeDtypeStruct((N_FOX_HEADS, 1, S), F32)],
                  compiler_params=_params("parallel", "arbitrary"))(i_tab, j_tab, qp, kp, vp, dop)


def _layer_step(x, tgt, w, p, late_weights=None, grad_sink=None, after=None, first_weights=None):
    S = x.shape[0]
    after_norm, after_proj = after if after is not None else (None, None)
    h = _rms_fwd(x, p["norm_mix_g"], name="rms_mix", after=after_norm)
    if first_weights is not None:
        w = {**w, **first_weights(h)}
    qkv = _mm(h, w["qkv"][:, 3 * DIL_WIDTH:], name="proj_fox", out_dtype=_CD, tn=768, tm=2048, after=after_proj)
    dil_qkv = _proj_dil(h, w["qkv"], name="proj_dil")
    zf = _mm(h, w["f"], name="proj_f")
    gl = _mm(h, w["g"], name="proj_gate", tn=1024, out_dtype=_CD)

    dil_o, dil_l = [], []
    for g in range(N_DIL_GROUPS):
        og, lg = _dil_fwd(dil_qkv[g], g, name=f"dil_fwd{g}")
        dil_o.append(og), dil_l.append(lg)
    o_a = _dil_mix_fwd(dil_o, dil_l, name="dil_mix")

    c = _fox_cumsum(zf, p["b_fgt"], name="fox_cumsum")
    fqp, fkp, fvp = _fox_pack_fwd(qkv, c, name="fox_pack")
    o_b, flse = _fox_fwd(fqp, fkp, fvp, name="fox_fwd")

    if late_weights is not None:
        w = {**w, **late_weights(o_b)}
    y_a = _mm(o_a, w["dil_out"], name="y_a", tn=1024, out_dtype=_CD)
    y_b = _mm(o_b, w["fox_out"], name="y_b", tn=1024, out_dtype=_CD)
    merged, x1, h2 = _gated_mix_out(gl, p["b_gate"], y_a, y_b, w["out"], x, p["norm_ffn_g"], name="mix_out")
    gate, up, act = _ffn_in_act(h2, w["ffn_in"], name="ffn_in")
    loss, dx2, dg_final = _ffn_down_loss(act, w["ffn_down"], x1, p["norm_final_g"], tgt, name="ffn_down_loss")

    gw_ffn_down = _mm(act, dx2, name="gw_ffn_down", ta=True, out_dtype=_CD, tm=1408)
    dgu = _d_swiglu(dx2, w["ffn_down"], gate, up, name="d_swiglu")
    gw_ffn_in = _mm(h2, dgu, name="gw_ffn_in", ta=True, out_dtype=_CD, tn=1408, out_blocks=1408, b_halves=True)
    sink = grad_sink if grad_sink is not None else (lambda group, grads: None)
    tok = sink("ffn", dict(ffn_in=gw_ffn_in, ffn_down=gw_ffn_down))
    dx1, dg_ffn = _d_h2(dgu, w["ffn_in"], x1, p["norm_ffn_g"], dx2, name="d_h2", after=tok)

    gw_out = _mm(merged, dx1, name="gw_out", ta=True, out_dtype=_CD, tk=S)
    dy_a, dy_b, dgl, db_gate = _gate_bwd(dx1, w["out"], gl, p["b_gate"], y_a, y_b, name="gate_bwd")
    do_a = _mm(dy_a, w["dil_out"], name="d_o_a", tb=True)
    gw_dil_out = _mm(o_a, dy_a, name="gw_dil_out", ta=True, out_dtype=_CD, tn=1024)
    do_b = _mm(dy_b, w["fox_out"], name="d_o_b", tb=True)
    gw_fox_out = _mm(o_b, dy_b, name="gw_fox_out", ta=True, out_dtype=_CD, tn=1024)
    tok = sink("mix", dict(dil_out=gw_dil_out, fox_out=gw_fox_out, out=gw_out))

    bqp, bdop = _fox_pack_bwd(qkv, c, o_b, flse, do_b, name="fox_pack_bwd", after=tok)
    dqp, dkp, dvp, dck, dcq = _fox_bwd(bqp, fkp, fvp, bdop, name="fox_bwd")
    dc = dcq[:, 0, :].T - dck.reshape(S, N_FOX_HEADS, HEAD_DIM)[:, :, 0]
    dc = jnp.pad(dc, ((0, 0), (0, F_PAD - N_FOX_HEADS)))
    dzf, db_fgt = _fox_cumsum_bwd(dc, zf, p["b_fgt"], name="fox_cumsum_bwd")

    douts = _dil_mix_bwd(do_a, dil_o, dil_l, name="dil_mix_bwd", after=tok)
    dqs, dks, dvs = [], [], []
    for g in range(N_DIL_GROUPS):
        dq, dk, dv = _dil_bwd(dil_qkv[g], douts[3 + g], douts[g], g, name=f"dil_bwd{g}")
        for parts, t in ((dqs, dq), (dks, dk), (dvs, dv)):
            parts.extend([t[0].astype(_CD), t[1].astype(_CD)])
    dqkv = jnp.concatenate(dqs + dks + dvs + [dqp, dkp, dvp], axis=1)

    gw_qkv = _mm(h, dqkv, name="gw_qkv", ta=True, out_dtype=_CD, tn=768, tk=S)
    gw_g = _mm(h, dgl, name="gw_gate", ta=True, out_dtype=_CD, tk=S)
    gw_f = _mm(h, dzf, name="gw_f", ta=True, out_dtype=_CD)
    tok = sink("in", dict(qkv=gw_qkv, f=gw_f, g=gw_g))
    dx, dg_mix = _mm(dqkv, w["qkv"], name="d_h", tb=True, tk=QKV_COLS, tm=256, more=((dgl, w["g"]), (dzf, w["f"])),
                     rms_bwd=(x, p["norm_mix_g"], dx1), after=tok)

    gw = dict(qkv=gw_qkv, f=gw_f, g=gw_g, dil_out=gw_dil_out, fox_out=gw_fox_out, out=gw_out, ffn_in=gw_ffn_in,
              ffn_down=gw_ffn_down)
    small = dict(norm_mix_g=dg_mix, b_fgt=db_fgt, b_gate=db_gate, norm_ffn_g=dg_ffn, norm_final_g=dg_final)
    return loss, dx, gw, small


def _position():
    return lax.axis_index("x"), lax.axis_index("y"), lax.axis_index("c")


def _other_chips(x, y):
    return [(1 - x, y), (x, 1 - y), (1 - x, 1 - y)]


ROW_TILE = 16


def _row_chunks(rows, want=4):
    n = want
    while n > 1 and rows % (n * ROW_TILE):
        n //= 2
    return n


SEM_SPEC = pl.BlockSpec(memory_space=pltpu.SEMAPHORE)
ANY_SPEC = pl.BlockSpec(memory_space=pl.ANY)
DATAFLOW = pltpu.SideEffectType.DATAFLOW_SIDE_EFFECTING


def _in_hbm(a):
    return pltpu.with_memory_space_constraint(a, pltpu.HBM)


def _split_copy_start(srcs, land_shapes, copies, after, *, name):
    n, m = len(srcs), len(land_shapes)

    def body(*refs):
        src_refs, land_refs = refs[:n], refs[n:n + m]
        send_sems, recv_sems = refs[n + m + 1], refs[n + m + 2]
        token = refs[-1]
        x, y, c = _position()
        for k, (src, dst, peer) in enumerate(copies(x, y, c, src_refs, land_refs)):
            pltpu.make_async_remote_copy(src_ref=src, dst_ref=dst, send_sem=send_sems.at[k], recv_sem=recv_sems.at[k],
                                         device_id=peer, device_id_type=MESH).start()
        token[...] = jnp.zeros_like(token)

    lands = [lax.empty(s.shape, s.dtype) for s in land_shapes]
    count = len(copies(0, 0, 0, srcs, lands))
    out = _pcall(
        body, name=name,
        out_shape=(pltpu.SemaphoreType.DMA((count,)), pltpu.SemaphoreType.DMA((count,)),
                   *[pltpu.HBM(s.shape, s.dtype) for s in srcs], *[pltpu.HBM(s.shape, s.dtype) for s in land_shapes],
                   jax.ShapeDtypeStruct((8, 128), F32)),
        in_specs=[HBM_SPEC] * (n + m) + [ANY_SPEC],
        out_specs=(SEM_SPEC, SEM_SPEC, *[HBM_SPEC] * (n + m), pl.BlockSpec(memory_space=pltpu.VMEM)),
        input_output_aliases={k: 2 + k for k in range(n + m)},
        compiler_params=pltpu.CompilerParams(has_side_effects=DATAFLOW),
    )(*[_in_hbm(s) for s in srcs], *[_in_hbm(l) for l in lands], after)
    return out[0], out[1], list(out[2:2 + n]), list(out[2 + n:2 + n + m]), out[-1]


def _split_copy_wait(send_sems, recv_sems, srcs, lands, copies, after, *, name):
    n, m = len(srcs), len(lands)

    def body(*refs):
        src_refs, land_refs = refs[:n], refs[n:n + m]
        send, recv = refs[n + m], refs[n + m + 1]
        x, y, c = _position()
        for k, (src, dst, peer) in enumerate(copies(x, y, c, src_refs, land_refs)):
            cp = pltpu.make_async_remote_copy(src_ref=src, dst_ref=dst, send_sem=send.at[k], recv_sem=recv.at[k],
                                              device_id=peer, device_id_type=MESH)
            cp.wait_send()
            cp.wait_recv()

    afters = list(after) if isinstance(after, (list, tuple)) else [after]
    out = _pcall(
        body, name=name,
        out_shape=tuple(pltpu.HBM(s.shape, s.dtype) for s in list(srcs) + list(lands)),
        in_specs=[HBM_SPEC] * (n + m) + [SEM_SPEC, SEM_SPEC] + [ANY_SPEC] * len(afters),
        out_specs=tuple([HBM_SPEC] * (n + m)),
        input_output_aliases={k: k for k in range(n + m)},
        compiler_params=pltpu.CompilerParams(has_side_effects=DATAFLOW),
    )(*srcs, *lands, send_sems, recv_sems, *afters)
    return list(out[:n]), list(out[n:])


def _gather_copies(x, y, c, shard_refs, land_refs):
    out = []
    for s, l in zip(shard_refs, land_refs):
        half = s.shape[0] // 2
        nq = _row_chunks(half)
        for cx, cy in _other_chips(x, y):
            for q in range(nq):
                rows = pl.ds(c * half + q * (half // nq), half // nq)
                out.append((s.at[rows, :], l.at[2 * x + y, rows, :], (cx, cy, c)))
    return out


def _gather_whole_copies(x, y, c, shard_refs, land_refs):
    out = []
    for s, l in zip(shard_refs, land_refs):
        nq = _row_chunks(s.shape[0])
        for cx, cy in _other_chips(x, y):
            for q in range(nq):
                rows = pl.ds(q * (s.shape[0] // nq), s.shape[0] // nq)
                out.append((s.at[rows, :], l.at[2 * x + y, rows, :], (cx, cy, c)))
    return out


def _scatter_all_copies(x, y, c, block_refs, land_refs):
    out = []
    for g, l in zip(block_refs, land_refs):
        half = g.shape[1] // 2
        nq = _row_chunks(half)
        size = half // nq
        for q in range(nq):
            rows = pl.ds((1 - c) * half + q * size, size)
            out.append((g.at[2 * x + y, rows, :], l.at[0, pl.ds(q * size, size), :], (x, y, 1 - c)))
        for r, (cx, cy) in enumerate(_other_chips(x, y)):
            for j in range(2):
                h = c if j == 0 else 1 - c
                for q in range(nq):
                    rows = pl.ds(h * half + q * size, size)
                    out.append((g.at[2 * cx + cy, rows, :], l.at[1 + 2 * r + j, pl.ds(q * size, size), :], (cx, cy, h)))
    return out


def _forward_halves(lands, *, name):
    n = len(lands)

    def body(*refs):
        ins = refs[:n]
        send_sems, recv_sems = refs[2 * n:]
        x, y, c = _position()
        copies = []
        for w in range(n):
            half = ins[w].shape[1] // 2
            for r, (cx, cy) in enumerate(_other_chips(x, y)):
                blk = ins[w].at[2 * cx + cy, pl.ds(c * half, half), :]
                cp = pltpu.make_async_remote_copy(src_ref=blk, dst_ref=blk, send_sem=send_sems.at[w, r],
                                                  recv_sem=recv_sems.at[w, r], device_id=(x, y, 1 - c),
                                                  device_id_type=MESH)
                cp.start()
                copies.append(cp)
        for w in range(n):
            half = ins[w].shape[1] // 2
            for r, (cx, cy) in enumerate(_other_chips(x, y)):
                blk = ins[w].at[2 * cx + cy, pl.ds((1 - c) * half, half), :]
                pltpu.make_async_remote_copy(src_ref=blk, dst_ref=blk, send_sem=send_sems.at[w, r],
                                             recv_sem=recv_sems.at[w, r], device_id=(x, y, 1 - c),
                                             device_id_type=MESH).wait_recv()
        for cp in copies:
            cp.wait_send()

    return _pcall(
        body, name=name, in_specs=[HBM_SPEC] * n, out_specs=[HBM_SPEC] * n,
        out_shape=[jax.ShapeDtypeStruct(l.shape, l.dtype) for l in lands],
        input_output_aliases={k: k for k in range(n)},
        scratch_shapes=[pltpu.SemaphoreType.DMA((n, 3)), pltpu.SemaphoreType.DMA((n, 3))],
    )(*lands)


def _share_halves(halves, *, name):
    n = len(halves)

    def body(*refs):
        ins, outs = refs[:n], refs[n:2 * n]
        send_sems, recv_sems = refs[2 * n:]
        x, y, c = _position()
        copies = []
        for w in range(n):
            cp = pltpu.make_async_remote_copy(src_ref=ins[w], dst_ref=outs[w], send_sem=send_sems.at[w],
                                              recv_sem=recv_sems.at[w], device_id=(x, y, 1 - c), device_id_type=MESH)
            cp.start()
            copies.append(cp)
        for cp in copies:
            cp.wait()

    return _pcall(
        body, name=name, in_specs=[HBM_SPEC] * n, out_specs=[HBM_SPEC] * n,
        out_shape=[jax.ShapeDtypeStruct(h.shape, h.dtype) for h in halves],
        scratch_shapes=[pltpu.SemaphoreType.DMA((n,)), pltpu.SemaphoreType.DMA((n,))],
    )(*halves)


def _sum_small(part, after=None):
    rows, width = part.shape

    def body(x_ref, out_ref, all_ref, send_sems, recv_sems):
        x, y, c = _position()
        me, sibling = (x, y, c), (x, y, 1 - c)
        chips = _other_chips(x, y)

        def block(px, py, pc):
            return all_ref.at[pl.ds((4 * px + 2 * py + pc) * rows, rows), :]

        def copy(k, blk, to, src=None):
            return pltpu.make_async_remote_copy(
                src_ref=block(*blk) if src is None else src, dst_ref=block(*blk), send_sem=send_sems.at[k],
                recv_sem=recv_sems.at[k], device_id=to, device_id_type=MESH)

        all_ref[pl.ds((4 * x + 2 * y + c) * rows, rows), :] = x_ref[...]
        first = [copy(0, me, sibling, src=x_ref)]
        first += [copy(1 + j, me, (*chip, c), src=x_ref) for j, chip in enumerate(chips)]
        for cp in first:
            cp.start()
        passed = [copy(4 + j, (*chip, c), sibling) for j, chip in enumerate(chips)]
        for j, chip in enumerate(chips):
            copy(1 + j, (*chip, c), me).wait_recv()
            passed[j].start()
        copy(0, sibling, me).wait_recv()
        for j, chip in enumerate(chips):
            copy(4 + j, (*chip, 1 - c), me).wait_recv()
        for cp in first + passed:
            cp.wait_send()
        total = all_ref[0:rows, :]
        for d in range(1, 8):
            total = total + all_ref[d * rows:(d + 1) * rows, :]
        out_ref[...] = total

    vm = pl.BlockSpec(memory_space=pltpu.VMEM)
    return _pcall(
        body, after, name="sum_small", in_specs=[vm], out_specs=vm, out_shape=jax.ShapeDtypeStruct((rows, width), F32),
        scratch_shapes=[pltpu.VMEM((8 * rows, width), F32), pltpu.SemaphoreType.DMA((7,)), pltpu.SemaphoreType.DMA((7,))],
    )(part)


def _row_tile(R, C, itemsize=4, budget=1 << 20):
    for t in (512, 256, 128, 64, 32, 16, 8):
        if R % t == 0 and t * C * itemsize <= budget:
            return t
    return R


def _add_all(g, recv, where, *, name):
    _, R, C = g.shape
    half = R // 2
    t = _row_tile(half, C)
    nb = half // t

    def body(w_ref, g_ref, r_ref, o_ref):
        total = g_ref[0].astype(F32)
        for k in range(7):
            total = total + r_ref[k].astype(F32)
        o_ref[...] = total

    grid_spec = pltpu.PrefetchScalarGridSpec(
        num_scalar_prefetch=1, grid=(nb,),
        in_specs=[pl.BlockSpec((1, t, C), lambda i, wr: (wr[0], wr[1] * nb + i, 0)),
                  pl.BlockSpec((7, t, C), lambda i, wr: (0, i, 0))],
        out_specs=pl.BlockSpec((t, C), lambda i, wr: (i, 0)))
    return _pcall(body, name=name, grid_spec=grid_spec, out_shape=jax.ShapeDtypeStruct((half, C), F32),
                  compiler_params=_params("parallel"))(where, g, recv)


def _adamw(w, g, m, v, *, name):
    R, C = w.shape
    t = _row_tile(R, C)
    c1 = 1.0 - ADAM_B1 ** ADAM_STEP
    c2 = 1.0 - ADAM_B2 ** ADAM_STEP

    def body(w_ref, g_ref, m_ref, v_ref, d_ref, nm_ref, nv_ref):
        gv = g_ref[...]
        mn = ADAM_B1 * m_ref[...] + (1.0 - ADAM_B1) * gv
        vn = ADAM_B2 * v_ref[...] + (1.0 - ADAM_B2) * (gv * gv)
        d_ref[...] = -ADAM_LR * ((mn / c1) / (jnp.sqrt(vn / c2) + ADAM_EPS) + ADAM_WD * w_ref[...])
        nm_ref[...] = mn
        nv_ref[...] = vn

    blk = pl.BlockSpec((t, C), lambda i: (i, 0))
    shp = jax.ShapeDtypeStruct((R, C), F32)
    return _pcall(body, name=name, grid=(R // t,), in_specs=[blk] * 4, out_specs=[blk] * 3, out_shape=[shp] * 3,
                  compiler_params=_params("parallel"))(w, g, m, v)


def _adamw_halves(w, mine, theirs, m, v, core, *, name, after=None):
    R, C = w.shape
    half = R // 2
    t = _row_tile(half, C)
    nbh = half // t
    c1 = 1.0 - ADAM_B1 ** ADAM_STEP
    c2 = 1.0 - ADAM_B2 ** ADAM_STEP

    def body(core_ref, w_ref, a_ref, b_ref, m_ref, v_ref, *rest):
        g_ref, d_ref, nm_ref, nv_ref = rest[-4:]
        gv = jnp.where(pl.program_id(0) // nbh == core_ref[0], a_ref[...], b_ref[...])
        mn = ADAM_B1 * m_ref[...] + (1.0 - ADAM_B1) * gv
        vn = ADAM_B2 * v_ref[...] + (1.0 - ADAM_B2) * (gv * gv)
        g_ref[...] = gv
        d_ref[...] = -ADAM_LR * ((mn / c1) / (jnp.sqrt(vn / c2) + ADAM_EPS) + ADAM_WD * w_ref[...])
        nm_ref[...] = mn
        nv_ref[...] = vn

    blk = pl.BlockSpec((t, C), lambda i, cr: (i, 0))
    hblk = pl.BlockSpec((t, C), lambda i, cr: (i % nbh, 0))
    shp = jax.ShapeDtypeStruct((R, C), F32)
    tied = [] if after is None else [after]
    grid_spec = pltpu.PrefetchScalarGridSpec(num_scalar_prefetch=1, grid=(2 * nbh,),
                                             in_specs=[blk, hblk, hblk, blk, blk] + [ANY_SPEC] * len(tied),
                                             out_specs=[blk] * 4)
    return _pcall(body, name=name, grid_spec=grid_spec, out_shape=[shp] * 4,
                  compiler_params=_params("parallel"))(core, w, mine, theirs, m, v, *tied)


BIG = ("w_in", "w_dil_out", "w_fox_out", "w_out", "w_ffn_in", "w_ffn_down")
SMALL = ("norm_mix_g", "b_fgt", "b_gate", "norm_ffn_g", "norm_final_g")
ORDER = ("norm_mix_g", "w_in", "b_fgt", "b_gate", "w_dil_out", "w_fox_out", "w_out", "norm_ffn_g", "w_ffn_in",
         "w_ffn_down", "norm_final_g")
SMALL_ROWS = {"norm_mix_g": (0, 1), "b_gate": (1, 3), "norm_ffn_g": (3, 4), "norm_final_g": (4, 5), "b_fgt": (5, 6)}


def _columns_to_blocks(full, ncol):
    K = full.shape[0]
    return full.reshape(K, 4, ncol).transpose(1, 0, 2)


def _pieces_to_blocks(pieces, ncol):
    spans, start = [], 0
    for piece in pieces:
        spans.append((piece, start, start + piece.shape[1]))
        start += piece.shape[1]
    assert start == 4 * ncol
    blocks = []
    for k in range(4):
        lo, hi = k * ncol, (k + 1) * ncol
        parts = [p[:, max(lo, a) - a:min(hi, b) - a] for p, a, b in spans if a < hi and b > lo]
        blocks.append(parts[0] if len(parts) == 1 else jnp.concatenate(parts, axis=1))
    return jnp.stack(blocks)


def _blocks_to_pieces(blocks, widths):
    n, K, ncol = blocks.shape
    assert sum(widths) == n * ncol
    pieces, lo = [], 0
    for width in widths:
        hi = lo + width
        parts = [blocks[k][:, max(lo, k * ncol) - k * ncol:min(hi, (k + 1) * ncol) - k * ncol]
                 for k in range(n) if k * ncol < hi and (k + 1) * ncol > lo]
        pieces.append(parts[0] if len(parts) == 1 else jnp.concatenate(parts, axis=1))
        lo = hi
    return pieces


def _blocks_to_columns(blocks):
    n, K, ncol = blocks.shape
    return blocks.transpose(1, 0, 2).reshape(K, n * ncol)


def kernel(x, norm_mix_g, w_in, b_fgt, b_gate, w_dil_out, w_fox_out, w_out, norm_ffn_g, w_ffn_in, w_ffn_down, norm_final_g, loss_target, m_norm_mix_g, m_w_in, m_b_fgt, m_b_gate, m_w_dil_out, m_w_fox_out, m_w_out, m_norm_ffn_g, m_w_ffn_in, m_w_ffn_down, m_norm_final_g, v_norm_mix_g, v_w_in, v_b_fgt, v_b_gate, v_w_dil_out, v_w_fox_out, v_w_out, v_norm_ffn_g, v_w_ffn_in, v_w_ffn_down, v_norm_final_g):
    weights = dict(norm_mix_g=norm_mix_g, w_in=w_in, b_fgt=b_fgt, b_gate=b_gate, w_dil_out=w_dil_out,
                   w_fox_out=w_fox_out, w_out=w_out, norm_ffn_g=norm_ffn_g, w_ffn_in=w_ffn_in, w_ffn_down=w_ffn_down,
                   norm_final_g=norm_final_g)
    m_in = dict(norm_mix_g=m_norm_mix_g, w_in=m_w_in, b_fgt=m_b_fgt, b_gate=m_b_gate, w_dil_out=m_w_dil_out,
                w_fox_out=m_w_fox_out, w_out=m_w_out, norm_ffn_g=m_norm_ffn_g, w_ffn_in=m_w_ffn_in,
                w_ffn_down=m_w_ffn_down, norm_final_g=m_norm_final_g)
    v_in = dict(norm_mix_g=v_norm_mix_g, w_in=v_w_in, b_fgt=v_b_fgt, b_gate=v_b_gate, w_dil_out=v_w_dil_out,
                w_fox_out=v_w_fox_out, w_out=v_w_out, norm_ffn_g=v_norm_ffn_g, w_ffn_in=v_w_ffn_in,
                w_ffn_down=v_w_ffn_down, norm_final_g=v_norm_final_g)
    c = lax.axis_index("c")
    chip = 2 * lax.axis_index("x") + lax.axis_index("y")

    shards = {n: weights[n][0].astype(_CD) for n in BIG}
    in_shape = jax.ShapeDtypeStruct((4,) + shards["w_in"].shape, _CD)
    send_i, recv_i, in_src, in_land, token_in = _split_copy_start(
        [shards["w_in"]], [in_shape], _gather_copies, norm_mix_g, name="gather_in_start")
    late = BIG[1:]
    send_g, recv_g, late_src, late_land, token = _split_copy_start(
        [shards[n] for n in late], [jax.ShapeDtypeStruct((4,) + shards[n].shape, _CD) for n in late],
        _gather_whole_copies, token_in, name="gather_late_start")
    adam_in = [t[0] + token_in[0, 0] for t in (w_in, m_w_in, v_w_in)]
    p = dict(norm_mix_g=norm_mix_g, b_fgt=jnp.pad(b_fgt, ((0, 0), (0, F_PAD - N_FOX_HEADS))), b_gate=b_gate,
             norm_ffn_g=norm_ffn_g, norm_final_g=norm_final_g.reshape(1, D_MODEL))

    def first_weights(after):
        own, lands = _split_copy_wait(send_i, recv_i, in_src, in_land, _gather_copies, [after] + adam_in,
                                      name="gather_in_wait")
        (g_in,) = _forward_halves(lands, name="gather_in_forward")
        blocks = lax.dynamic_update_index_in_dim(g_in, own[0], chip, 0)
        qkv, f, g = _blocks_to_pieces(blocks, (QKV_COLS, N_FOX_HEADS, 2 * D_MODEL))
        return dict(qkv=qkv, f=jnp.pad(f, ((0, 0), (0, F_PAD - N_FOX_HEADS))), g=g)

    def late_weights(after):
        own, lands = _split_copy_wait(send_g, recv_g, late_src, late_land, _gather_whole_copies, after,
                                      name="gather_late_wait")
        g_dil, g_fox, g_out, g_ffn_in, g_ffn_down = [
            lax.dynamic_update_index_in_dim(l, s, chip, 0) for l, s in zip(lands, own)]
        return dict(dil_out=_blocks_to_columns(g_dil), fox_out=_blocks_to_columns(g_fox),
                    out=g_out.reshape(D_MODEL, D_MODEL), ffn_in=g_ffn_in,
                    ffn_down=g_ffn_down.reshape(D_FF, D_MODEL))

    def to_blocks(n, full):
        shape = weights[n].shape
        if full.ndim == 3:
            return full
        if n in ("w_out", "w_ffn_down"):
            return full.reshape(4, shape[1], shape[2])
        return _columns_to_blocks(full, shape[2])

    in_flight = {}

    def grad_sink(group, gw):
        if group == "in":
            named = {"w_in": _pieces_to_blocks([gw["qkv"], gw["f"][:, :N_FOX_HEADS], gw["g"]], weights["w_in"].shape[2])}
        else:
            named = {"w_" + k: v for k, v in gw.items()}
        srcs = [to_blocks(n, named[n]) for n in named]
        lands = [jax.ShapeDtypeStruct((7, s.shape[1] // 2, s.shape[2]), s.dtype) for s in srcs]
        started = _split_copy_start(srcs, lands, _scatter_all_copies, next(iter(gw.values())),
                                    name=f"scatter_{group}_start")
        in_flight[group] = (list(named), started)
        return started[-1]

    loss_part, grad_x, gw, small = _layer_step(x[0], loss_target[0], {}, p, late_weights, grad_sink,
                                               (token_in, token), first_weights)

    where = jnp.stack([chip, c]).astype(jnp.int32)

    def summed_halves(groups, after, name):
        halves = {}
        for group in groups:
            names, (send_s, recv_s, srcs, lands, _) = in_flight[group]
            srcs, recv = _split_copy_wait(send_s, recv_s, srcs, lands, _scatter_all_copies, after,
                                          name=f"scatter_{group}_wait")
            halves.update({n: _add_all(s, r, where, name=f"add_all_{n}") for n, s, r in zip(names, srcs, recv)})
        return {n: (h, o) for (n, h), o in zip(halves.items(), _share_halves(list(halves.values()), name=name))}

    grad_halves = summed_halves(("ffn", "mix"), grad_x, "share_halves")

    out_g, out_d, out_m, out_v = {}, {}, {}, {}
    core = jnp.reshape(c, (1,)).astype(jnp.int32)

    def adamw_big(n, after=None):
        shape = weights[n].shape
        wmv = adam_in if n == "w_in" else [t[0] for t in (weights[n], m_in[n], v_in[n])]
        mine, theirs = grad_halves[n]
        outs = _adamw_halves(wmv[0], mine, theirs, wmv[1], wmv[2], core, name=f"adamw_{n}", after=after)
        out_g[n], out_d[n], out_m[n], out_v[n] = [t.reshape(shape) for t in outs]

    early = ("w_dil_out", "w_fox_out", "w_out", "w_ffn_down")
    for n in early:
        adamw_big(n)
    grad_halves.update(summed_halves(("in",), [grad_x] + [out_d[n] for n in early], "share_halves_in"))
    adamw_big("w_in")

    packed = jnp.concatenate([
        small["norm_mix_g"], small["b_gate"].reshape(2, D_MODEL), small["norm_ffn_g"], small["norm_final_g"],
        jnp.pad(small["b_fgt"], ((0, 0), (0, D_MODEL - F_PAD))), jnp.pad(loss_part, ((0, 0), (0, D_MODEL - 1))),
        jnp.zeros((1, D_MODEL), F32)], axis=0)
    summed = _sum_small(packed, after=out_d["w_in"])
    loss = summed[6, 0]
    adamw_big("w_ffn_in", after=summed)

    for n in SMALL:
        lo, hi = SMALL_ROWS[n]
        shape = weights[n].shape
        g2 = summed[lo:hi].reshape(1, -1)[:, :weights[n].size]
        d2, m2, v2 = _adamw(weights[n].reshape(g2.shape), g2, m_in[n].reshape(g2.shape), v_in[n].reshape(g2.shape),
                            name=f"adamw_{n}")
        out_g[n], out_d[n], out_m[n], out_v[n] = [t.reshape(shape) for t in (g2, d2, m2, v2)]
    return (loss, grad_x[None], *[out_g[n] for n in ORDER], *[out_d[n] for n in ORDER],
            *[out_m[n] for n in ORDER], *[out_v[n] for n in ORDER])
```

```python
import numpy as np
import jax
import jax.numpy as jnp
from jax import lax
from jax.experimental import pallas as pl
from jax.experimental.pallas import tpu as pltpu

F32 = jnp.float32
_CD = jnp.bfloat16

D_MODEL = 1024
HEAD_DIM = 64
DIL_PAIRS = ((128, 1), (512, 4), (2048, 16))
N_DIL_GROUPS = 3
DIL_HEADS = 4
DIL_W = 128
DIL_OUT = DIL_HEADS * HEAD_DIM
DIL_WIDTH = N_DIL_GROUPS * DIL_OUT
N_FOX_HEADS = 8
FOX_WIDTH = N_FOX_HEADS * HEAD_DIM
D_FF = 2816
QKV_COLS = 3 * DIL_WIDTH + 3 * FOX_WIDTH
F_PAD = 128
RMS_EPS = 1e-6
NEG_INF = -1e30
ATTN_SCALE = HEAD_DIM ** -0.5
ADAM_LR, ADAM_B1, ADAM_B2, ADAM_EPS, ADAM_WD, ADAM_STEP = 0.001, 0.9, 0.999, 1e-08, 0.01, 10

VMEM_LIMIT = 48 * 1024 * 1024
VMEM_LIMIT_RESIDENT = 56 * 1024 * 1024
LANES = 128
MESH = pl.DeviceIdType.MESH
HBM_SPEC = pl.BlockSpec(memory_space=pltpu.HBM)


def _pcall(body, after=None, **kw):
    if after is None:
        return pl.pallas_call(body, **kw)
    n_in = len(kw["in_specs"])
    kw["in_specs"] = list(kw["in_specs"]) + [pl.BlockSpec(memory_space=pl.ANY)]

    def tied(*refs):
        return body(*refs[:n_in], *refs[n_in + 1:])

    call = pl.pallas_call(tied, **kw)
    return lambda *args: call(*args, after)


def _params(*sem):
    return pltpu.CompilerParams(dimension_semantics=sem, vmem_limit_bytes=VMEM_LIMIT)


def _pick(dim, pref):
    t = (min(pref, dim) // 128) * 128
    while t >= 128:
        if dim % t == 0:
            return t
        t -= 128
    return dim


def _rms_bwd_store(r, x_ref, g_ref, dres_ref, o_ref, dg_ref):
    xv = x_ref[...]
    rs = lax.rsqrt(jnp.mean(xv * xv, axis=-1, keepdims=True) + RMS_EPS)
    xh = xv * rs
    dxh = r * g_ref[...]
    o_ref[...] = dres_ref[...] + rs * (dxh - xh * jnp.mean(dxh * xh, axis=-1, keepdims=True))
    part = jnp.sum(r * xh, axis=0, keepdims=True)
    first = pl.program_id(0) == 0

    @pl.when(first)
    def _():
        dg_ref[...] = part

    @pl.when(jnp.logical_not(first))
    def _():
        dg_ref[...] += part


def _d_h2(dgu, w_blocks, x, g, dres, *, name, tm=256, after=None):
    _, S, F = dgu.shape
    n, D, C = w_blocks.shape
    assert n == 4 and F == 2 * C
    nt = (((1,), (1,)), ((), ()))

    def body(dg_ref, du_ref, w_ref, x_ref, g_ref, dres_ref, o_ref, dgn_ref):
        r = None
        for k in range(n):
            a_ref = dg_ref if k < 2 else du_ref
            p = lax.dot_general(a_ref[:, (k % 2) * C:(k % 2 + 1) * C].astype(_CD), w_ref[k].astype(_CD), nt,
                                preferred_element_type=F32)
            r = p if r is None else r + p
        _rms_bwd_store(r, x_ref, g_ref, dres_ref, o_ref, dgn_ref)

    row = pl.BlockSpec((tm, D), lambda i: (i, 0))
    vec = pl.BlockSpec((1, D), lambda i: (0, 0))
    return _pcall(
        body, after, name=name, grid=(S // tm,),
        in_specs=[pl.BlockSpec((None, tm, F), lambda i: (0, i, 0)), pl.BlockSpec((None, tm, F), lambda i: (1, i, 0)),
                  pl.BlockSpec((n, D, C), lambda i: (0, 0, 0)), row, vec, row],
        out_specs=[row, vec], out_shape=[jax.ShapeDtypeStruct((S, D), F32), jax.ShapeDtypeStruct((1, D), F32)],
        compiler_params=_params("arbitrary"))(dgu, dgu, w_blocks, x, g, dres)


def _mm(a, b, *, name, ta=False, tb=False, out_dtype=F32, tm=1024, tn=512, tk=2048, after=None,
        out_blocks=None, b_halves=False, rms_bwd=None, more=()):
    K, M = a.shape if ta else a.shape[::-1]
    b_rows, b_cols = (b.shape[1], 2 * b.shape[2]) if b_halves else b.shape
    if tb:
        N, K2 = b_rows, b_cols
    else:
        K2, N = b_rows, b_cols
    assert K == K2, (a.shape, b.shape)
    tm = _pick(M, tm)
    tn = _pick(out_blocks or N, tn)
    tk = _pick(K, tk)
    nk = K // tk
    dn = (((0 if ta else 1,), (1 if tb else 0,)), ((), ()))
    has_norm = rms_bwd is not None
    if has_norm:
        tn = N
        assert not out_blocks and out_dtype == F32
    assert not more or (nk == 1 and tb and not ta)

    def body(*refs):
        a_ref, b_ref = refs[0], refs[1]
        rest = list(refs[2:])
        more_refs = [(rest.pop(0), rest.pop(0)) for _ in more]
        x_ref, g_ref, dres_ref = (rest.pop(0), rest.pop(0), rest.pop(0)) if has_norm else (None, None, None)
        o_ref = rest.pop(0)
        dg_ref = rest.pop(0) if has_norm else None
        p = lax.dot_general(a_ref[...].astype(_CD), b_ref[...].astype(_CD), dn, preferred_element_type=F32)
        for a2_ref, b2_ref in more_refs:
            p += lax.dot_general(a2_ref[...].astype(_CD), b2_ref[...].astype(_CD), dn, preferred_element_type=F32)

        def finish(r):
            if has_norm:
                _rms_bwd_store(r, x_ref, g_ref, dres_ref, o_ref, dg_ref)
            elif out_blocks:
                o_ref[0] = r.astype(out_dtype)
            else:
                o_ref[...] = r.astype(out_dtype)

        if nk == 1:
            finish(p)
        else:
            acc_ref = rest.pop(0)
            k = pl.program_id(2)

            @pl.when(k == 0)
            def _():
                acc_ref[...] = p

            @pl.when(k > 0)
            def _():
                acc_ref[...] += p

            @pl.when(k == nk - 1)
            def _():
                finish(acc_ref[...])

    a_spec = pl.BlockSpec((tk, tm), lambda i, j, k: (k, i)) if ta else pl.BlockSpec((tm, tk), lambda i, j, k: (i, k))
    if b_halves:
        nb_ = (N // 2) // tn
        b_spec = pl.BlockSpec((None, tk, tn), lambda i, j, k: (j // nb_, k, j % nb_))
    else:
        b_spec = pl.BlockSpec((tn, tk), lambda i, j, k: (j, k)) if tb else pl.BlockSpec((tk, tn), lambda i, j, k: (k, j))
    if out_blocks:
        oper = out_blocks // tn
        o_spec = pl.BlockSpec((1, tm, tn), lambda i, j, k: (j // oper, i, j % oper))
        out_shape = jax.ShapeDtypeStruct((N // out_blocks, M, out_blocks), out_dtype)
    else:
        o_spec = pl.BlockSpec((tm, tn), lambda i, j, k: (i, j))
        out_shape = jax.ShapeDtypeStruct((M, N), out_dtype)
    in_specs, args = [a_spec, b_spec], (a, b)
    for a2, b2 in more:
        assert a2.shape[0] == M and b2.shape == (N, a2.shape[1]), (a2.shape, b2.shape)
        in_specs += [pl.BlockSpec((tm, a2.shape[1]), lambda i, j, k: (i, 0)),
                     pl.BlockSpec((tn, a2.shape[1]), lambda i, j, k: (j, 0))]
        args += (a2, b2)
    out_specs, semantics = o_spec, ("parallel", "parallel", "arbitrary")
    if has_norm:
        vec = pl.BlockSpec((1, N), lambda i, j, k: (0, 0))
        in_specs += [o_spec, vec, o_spec]
        args += tuple(rms_bwd)
        out_specs, out_shape = [o_spec, vec], [out_shape, jax.ShapeDtypeStruct((1, N), F32)]
        semantics = ("arbitrary", "arbitrary", "arbitrary")
    return _pcall(
        body, after, name=name, grid=(M // tm, N // tn, nk), in_specs=in_specs, out_specs=out_specs,
        out_shape=out_shape,
        scratch_shapes=[pltpu.VMEM((tm, tn), F32)] if nk > 1 else [],
        compiler_params=_params(*semantics),
    )(*args)


def _rms_fwd(x, g, *, name, tm=512, after=None):
    S, D = x.shape

    def body(x_ref, g_ref, h_ref):
        xv = x_ref[...]
        r = lax.rsqrt(jnp.mean(xv * xv, axis=-1, keepdims=True) + RMS_EPS)
        h_ref[...] = ((xv * r) * g_ref[...]).astype(h_ref.dtype)

    row = pl.BlockSpec((tm, D), lambda i: (i, 0))
    return _pcall(body, after, name=name, grid=(S // tm,), in_specs=[row, pl.BlockSpec((1, D), lambda i: (0, 0))],
                  out_specs=row, out_shape=jax.ShapeDtypeStruct((S, D), _CD), compiler_params=_params("parallel"))(x, g)


def _ffn_down_loss(act, w_down, x1, g, tgt, *, name, tm=512):
    S, D = x1.shape
    F = act.shape[1]

    def body(a_ref, b_ref, x_ref, g_ref, t_ref, loss_ref, dx_ref, dg_ref):
        xv = x_ref[...] + jnp.dot(a_ref[...].astype(_CD), b_ref[...].astype(_CD), preferred_element_type=F32)
        gv = g_ref[...]
        r = lax.rsqrt(jnp.mean(xv * xv, axis=-1, keepdims=True) + RMS_EPS)
        xh = xv * r
        err = xh * gv - t_ref[...]
        lpart = 0.5 * jnp.sum(jnp.mean(err * err, axis=-1, keepdims=True), axis=0, keepdims=True)
        dy = err * (1.0 / D)
        dxh = dy * gv
        dx_ref[...] = r * (dxh - xh * jnp.mean(dxh * xh, axis=-1, keepdims=True))
        gpart = jnp.sum(dy * xh, axis=0, keepdims=True)

        @pl.when(pl.program_id(0) == 0)
        def _():
            loss_ref[...] = lpart
            dg_ref[...] = gpart

        @pl.when(pl.program_id(0) > 0)
        def _():
            loss_ref[...] += lpart
            dg_ref[...] += gpart

    row = pl.BlockSpec((tm, D), lambda i: (i, 0))
    vec = pl.BlockSpec((1, D), lambda i: (0, 0))
    one = pl.BlockSpec((1, 1), lambda i: (0, 0))
    return _pcall(body, name=name, grid=(S // tm,),
                  in_specs=[pl.BlockSpec((tm, F), lambda i: (i, 0)), pl.BlockSpec((F, D), lambda i: (0, 0)), row, vec, row],
                  out_specs=[one, row, vec],
                  out_shape=[jax.ShapeDtypeStruct((1, 1), F32), jax.ShapeDtypeStruct((S, D), F32),
                             jax.ShapeDtypeStruct((1, D), F32)],
                  compiler_params=_params("arbitrary"))(act, w_down, x1, g, tgt)


def _sigmoid(z):
    return 1.0 / (1.0 + jnp.exp(-z))


def _gated_mix_out(gl, bg, ya, yb, w_out, x, g, *, name, tm=512):
    S, D = ya.shape

    def body(za_ref, zb_ref, ba_ref, bb_ref, ya_ref, yb_ref, w_ref, x_ref, g_ref, m_ref, x1_ref, h_ref):
        ga = _sigmoid(za_ref[...].astype(F32) + ba_ref[...])
        gb = _sigmoid(zb_ref[...].astype(F32) + bb_ref[...])
        merged = (ga * ya_ref[...].astype(F32) + gb * yb_ref[...].astype(F32)).astype(m_ref.dtype)
        m_ref[...] = merged
        x1 = x_ref[...] + jnp.dot(merged, w_ref[...].astype(_CD), preferred_element_type=F32)
        x1_ref[...] = x1
        rs = lax.rsqrt(jnp.mean(x1 * x1, axis=-1, keepdims=True) + RMS_EPS)
        h_ref[...] = ((x1 * rs) * g_ref[...]).astype(h_ref.dtype)

    lo = pl.BlockSpec((tm, D), lambda i: (i, 0))
    hi = pl.BlockSpec((tm, D), lambda i: (i, 1))
    vlo = pl.BlockSpec((1, D), lambda i: (0, 0))
    vhi = pl.BlockSpec((1, D), lambda i: (0, 1))
    whole = pl.BlockSpec((D, D), lambda i: (0, 0))
    return _pcall(body, name=name, grid=(S // tm,), in_specs=[lo, hi, vlo, vhi, lo, lo, whole, lo, vlo],
                  out_specs=[lo, lo, lo],
                  out_shape=[jax.ShapeDtypeStruct((S, D), _CD), jax.ShapeDtypeStruct((S, D), F32),
                             jax.ShapeDtypeStruct((S, D), _CD)],
                  compiler_params=_params("parallel"))(gl, gl, bg, bg, ya, yb, w_out, x, g)


def _gate_bwd(dx1, w_out, gl, bg, ya, yb, *, name, tm=512):
    S, D = ya.shape
    nt = (((1,), (1,)), ((), ()))

    def body(dx_ref, w_ref, za_ref, zb_ref, ba_ref, bb_ref, ya_ref, yb_ref, dya_ref, dyb_ref, dgl_ref, dbg_ref):
        dmv = lax.dot_general(dx_ref[...].astype(_CD), w_ref[...].astype(_CD), nt, preferred_element_type=F32)
        ga = _sigmoid(za_ref[...].astype(F32) + ba_ref[...])
        gb = _sigmoid(zb_ref[...].astype(F32) + bb_ref[...])
        dya_ref[...] = (dmv * ga).astype(dya_ref.dtype)
        dyb_ref[...] = (dmv * gb).astype(dyb_ref.dtype)
        dza = dmv * ya_ref[...].astype(F32) * ga * (1.0 - ga)
        dzb = dmv * yb_ref[...].astype(F32) * gb * (1.0 - gb)
        dgl_ref[:, :D] = dza.astype(dgl_ref.dtype)
        dgl_ref[:, D:] = dzb.astype(dgl_ref.dtype)
        pa = jnp.sum(dza, axis=0, keepdims=True)
        pb = jnp.sum(dzb, axis=0, keepdims=True)

        @pl.when(pl.program_id(0) == 0)
        def _():
            dbg_ref[:, :D] = pa
            dbg_ref[:, D:] = pb

        @pl.when(pl.program_id(0) > 0)
        def _():
            dbg_ref[:, :D] += pa
            dbg_ref[:, D:] += pb

    lo = pl.BlockSpec((tm, D), lambda i: (i, 0))
    hi = pl.BlockSpec((tm, D), lambda i: (i, 1))
    vlo = pl.BlockSpec((1, D), lambda i: (0, 0))
    vhi = pl.BlockSpec((1, D), lambda i: (0, 1))
    wide = pl.BlockSpec((tm, 2 * D), lambda i: (i, 0))
    vwide = pl.BlockSpec((1, 2 * D), lambda i: (0, 0))
    whole = pl.BlockSpec((D, D), lambda i: (0, 0))
    return _pcall(body, name=name, grid=(S // tm,), in_specs=[lo, whole, lo, hi, vlo, vhi, lo, lo],
                  out_specs=[lo, lo, wide, vwide],
                  out_shape=[jax.ShapeDtypeStruct((S, D), _CD), jax.ShapeDtypeStruct((S, D), _CD),
                             jax.ShapeDtypeStruct((S, 2 * D), _CD), jax.ShapeDtypeStruct((1, 2 * D), F32)],
                  compiler_params=_params("arbitrary"))(dx1, w_out, gl, gl, bg, bg, ya, yb)


def _ffn_in_act(h2, w_blocks, *, name, tm=512):
    S, D = h2.shape
    _, _, C = w_blocks.shape

    def body(a_ref, bg_ref, bu_ref, g_ref, u_ref, o_ref):
        av = a_ref[...].astype(_CD)
        gv = jnp.dot(av, bg_ref[0].astype(_CD), preferred_element_type=F32)
        uv = jnp.dot(av, bu_ref[0].astype(_CD), preferred_element_type=F32)
        g_ref[...] = gv.astype(g_ref.dtype)
        u_ref[...] = uv.astype(u_ref.dtype)
        o_ref[...] = (gv * _sigmoid(gv) * uv).astype(o_ref.dtype)

    out = pl.BlockSpec((tm, C), lambda i, j: (i, j))
    shp = jax.ShapeDtypeStruct((S, 2 * C), _CD)
    return _pcall(body, name=name, grid=(S // tm, 2),
                  in_specs=[pl.BlockSpec((tm, D), lambda i, j: (i, 0)), pl.BlockSpec((1, D, C), lambda i, j: (j, 0, 0)),
                            pl.BlockSpec((1, D, C), lambda i, j: (2 + j, 0, 0))],
                  out_specs=[out, out, out], out_shape=[shp, shp, shp],
                  compiler_params=_params("parallel", "arbitrary"))(h2, w_blocks, w_blocks)


def _d_swiglu(dx, w_down, gate, up, *, name, tm=512, tn=1408):
    S, D = dx.shape
    F = w_down.shape[0]
    nt = (((1,), (1,)), ((), ()))

    def body(a_ref, b_ref, g_ref, u_ref, o_ref):
        dv = lax.dot_general(a_ref[...].astype(_CD), b_ref[...].astype(_CD), nt, preferred_element_type=F32)
        gv = g_ref[...].astype(F32)
        sg = _sigmoid(gv)
        o_ref[0] = (dv * u_ref[...].astype(F32) * (sg * (1.0 + gv * (1.0 - sg)))).astype(o_ref.dtype)
        o_ref[1] = (dv * (gv * sg)).astype(o_ref.dtype)

    tile = pl.BlockSpec((tm, tn), lambda i, j: (i, j))
    return _pcall(body, name=name, grid=(S // tm, F // tn),
                  in_specs=[pl.BlockSpec((tm, D), lambda i, j: (i, 0)), pl.BlockSpec((tn, D), lambda i, j: (j, 0)),
                            tile, tile],
                  out_specs=pl.BlockSpec((2, tm, tn), lambda i, j: (0, i, j)),
                  out_shape=jax.ShapeDtypeStruct((2, S, F), _CD),
                  compiler_params=_params("parallel", "arbitrary"))(dx, w_down, gate, up)


def _split3(x):
    hi = x.astype(jnp.bfloat16)
    r1 = x - hi.astype(F32)
    mid = r1.astype(jnp.bfloat16)
    lo = (r1 - mid.astype(F32)).astype(jnp.bfloat16)
    return hi, mid, lo


def _ones_dot_left(ones, x):
    return sum(jnp.dot(ones, p, preferred_element_type=F32) for p in _split3(x))


def _ones_dot_right(x, ones):
    return sum(jnp.dot(p, ones, preferred_element_type=F32) for p in _split3(x))


def _head_sum(x):
    n = x.shape[1]
    r = lax.broadcasted_iota(jnp.int32, (n, n), 0) // HEAD_DIM
    c = lax.broadcasted_iota(jnp.int32, (n, n), 1) // HEAD_DIM
    return _ones_dot_right(x, (r == c).astype(jnp.bfloat16))


def _log_sigmoid(z):
    e = jnp.exp(-jnp.abs(z))
    t = 1.0 + e
    log1p_e = jnp.where(t == 1.0, e, jnp.log(t) * (e / jnp.where(t == 1.0, 1.0, t - 1.0)))
    return jnp.minimum(z, 0.0) - log1p_e


def _fox_cumsum(zf, bf, *, name):
    S, W = zf.shape
    nb = S // 128

    def body(z_ref, b_ref, c_ref):
        tri = (lax.broadcasted_iota(jnp.int32, (128, 128), 0) >= lax.broadcasted_iota(jnp.int32, (128, 128), 1))
        tri = tri.astype(jnp.bfloat16)

        def step(i, carry):
            rows = pl.ds(pl.multiple_of(i * 128, 128), 128)
            lf = _log_sigmoid(z_ref[rows, :] + b_ref[...])
            cb = _ones_dot_left(tri, lf) + carry
            c_ref[rows, :] = cb
            return cb[127:128, :]

        lax.fori_loop(0, nb, step, jnp.zeros((1, W), F32))

    return _pcall(body, name=name, out_shape=jax.ShapeDtypeStruct((S, W), F32),
                  compiler_params=pltpu.CompilerParams(vmem_limit_bytes=VMEM_LIMIT))(zf, bf)


def _fox_cumsum_bwd(dc, zf, bf, *, name):
    S, W = zf.shape
    nb = S // 128

    def body(dc_ref, z_ref, b_ref, dz_ref, db_ref):
        tri = (lax.broadcasted_iota(jnp.int32, (128, 128), 0) <= lax.broadcasted_iota(jnp.int32, (128, 128), 1))
        tri = tri.astype(jnp.bfloat16)

        def step(k, carry):
            tail, acc = carry
            i = nb - 1 - k
            rows = pl.ds(pl.multiple_of(i * 128, 128), 128)
            dlf = _ones_dot_left(tri, dc_ref[rows, :]) + tail
            dz = dlf * _sigmoid(-(z_ref[rows, :] + b_ref[...]))
            dz_ref[rows, :] = dz
            return dlf[0:1, :], acc + jnp.sum(dz, axis=0, keepdims=True)

        _, acc = lax.fori_loop(0, nb, step, (jnp.zeros((1, W), F32), jnp.zeros((1, W), F32)))
        db_ref[...] = acc

    return _pcall(body, name=name,
                  out_shape=[jax.ShapeDtypeStruct((S, W), F32), jax.ShapeDtypeStruct((1, W), F32)],
                  compiler_params=pltpu.CompilerParams(vmem_limit_bytes=VMEM_LIMIT))(dc, zf, bf)


def _proj_dil(h, w_qkv, *, name, tm=1024):
    S, D = h.shape
    tn = DIL_WIDTH

    def body(a_ref, b_ref, *rest):
        outs, acc = rest[:N_DIL_GROUPS], rest[N_DIL_GROUPS]
        prod = jnp.dot(a_ref[...].astype(_CD), b_ref[...].astype(_CD), preferred_element_type=F32)
        for k in range(tn // LANES):
            acc[k] = prod[:, k * LANES:(k + 1) * LANES]
        for g, (_, d) in enumerate(DIL_PAIRS):
            for half in range(DIL_OUT // LANES):
                k = g * (DIL_OUT // LANES) + half
                cols = slice(half * LANES, (half + 1) * LANES)
                for r in range(d):
                    rows = pl.ds(r, tm // d, stride=d) if d > 1 else slice(None)
                    outs[g][0, r, :, cols] = acc[k, rows, :].astype(outs[g].dtype)

    out_specs = [pl.BlockSpec((1, d, tm // d, DIL_OUT), lambda i, j: (j, 0, i, 0)) for _, d in DIL_PAIRS]
    out_shape = [jax.ShapeDtypeStruct((3, d, S // d, DIL_OUT), _CD) for _, d in DIL_PAIRS]
    outs = _pcall(body, name=name, grid=(S // tm, 3),
                  in_specs=[pl.BlockSpec((tm, D), lambda i, j: (i, 0)), pl.BlockSpec((D, tn), lambda i, j: (0, j))],
                  out_specs=out_specs, out_shape=out_shape, scratch_shapes=[pltpu.VMEM((tn // LANES, tm, LANES), F32)],
                  compiler_params=_params("parallel", "arbitrary"))(h, w_qkv)
    return [o.reshape(3, S, DIL_OUT) for o in outs]


def _dil_start(block, S, dilation):
    sub = S // dilation
    u0 = block * DIL_W
    return (u0 % sub) * dilation + u0 // sub


def _dil_slopes(group):
    h = np.arange(1, N_DIL_GROUPS * DIL_HEADS + 1, dtype=np.float32)
    s = (np.float32(2.0) ** (np.float32(-8.0) * h / np.float32(N_DIL_GROUPS * DIL_HEADS))).astype(np.float32)
    return [float(v) for v in s.reshape(N_DIL_GROUPS, DIL_HEADS)[group]]


def _dil_tiles(i, n, blocks_per_seq):
    qi = lax.broadcasted_iota(jnp.int32, (DIL_W, 2 * DIL_W), 0)
    kj = lax.broadcasted_iota(jnp.int32, (DIL_W, 2 * DIL_W), 1)
    rel = qi + DIL_W - kj
    first = ((4 * n + i) % blocks_per_seq) == 0
    valid = jnp.logical_and(jnp.logical_and(rel >= 0, rel <= DIL_W), jnp.logical_or(kj >= DIL_W, jnp.logical_not(first)))
    return valid, rel.astype(F32)


def _dil_window(cur_ref, prev_ref, i, cols):
    if i > 0:
        return cur_ref[(i - 1) * DIL_W:(i + 1) * DIL_W, cols]
    return jnp.concatenate([prev_ref[:, cols], cur_ref[:DIL_W, cols]], axis=0)


CHUNK = 4 * DIL_W


def _dil_rows(block, S, dilation):
    start = _dil_start(block, S, dilation)
    return pl.ds(start, DIL_W, stride=dilation) if dilation > 1 else pl.ds(start, DIL_W)


def SPLIT(S):
    return (DIL_OUT // LANES, S, LANES)


def _dil_fwd(qkv, group, *, name):
    S = qkv.shape[1]
    dilation = DIL_PAIRS[group][1]
    bps = (S // dilation) // DIL_W
    slopes = _dil_slopes(group)
    nt = (((1,), (1,)), ((), ()))

    def body(q_ref, k_ref, v_ref, kp_ref, vp_ref, on_ref, ln_ref, o_ref, l_ref):
        n = pl.program_id(0)
        for i in range(4):
            valid, rel = _dil_tiles(i, n, bps)
            rows = slice(i * DIL_W, (i + 1) * DIL_W)
            for h in range(DIL_HEADS):
                cols = slice(h * HEAD_DIM, (h + 1) * HEAD_DIM)
                qh = q_ref[rows, cols]
                k2, v2 = _dil_window(k_ref, kp_ref, i, cols), _dil_window(v_ref, vp_ref, i, cols)
                s = lax.dot_general(qh, k2, nt, preferred_element_type=F32) * ATTN_SCALE - (slopes[h] * dilation) * rel
                s = jnp.where(valid, s, NEG_INF)
                m = jnp.max(s, axis=-1, keepdims=True)
                p = jnp.exp(s - m)
                den = jnp.sum(p, axis=-1, keepdims=True)
                acc = jnp.dot(p.astype(_CD), v2, preferred_element_type=F32)
                o_ref[rows, cols] = acc / den
                l_ref[rows, cols] = jnp.broadcast_to(m + jnp.log(den), (DIL_W, HEAD_DIM))
        for i in range(4):
            rows = slice(i * DIL_W, (i + 1) * DIL_W)
            nat = _dil_rows(4 * n + i, S, dilation)
            for half in range(DIL_OUT // LANES):
                cols = slice(half * LANES, (half + 1) * LANES)
                on_ref[half, nat, :] = o_ref[rows, cols]
                ln_ref[half, nat, :] = l_ref[rows, cols]

    def cur(which):
        return pl.BlockSpec((None, CHUNK, DIL_OUT), lambda n: (which, n, 0))

    def prev(which):
        return pl.BlockSpec((None, DIL_W, DIL_OUT), lambda n: (which, jnp.maximum(4 * n - 1, 0), 0))

    whole = pl.BlockSpec(SPLIT(S), lambda n: (0, 0, 0))
    return _pcall(body, name=name, grid=(S // CHUNK,), in_specs=[cur(0), cur(1), cur(2), prev(1), prev(2)],
                  out_specs=[whole, whole],
                  out_shape=[jax.ShapeDtypeStruct(SPLIT(S), F32), jax.ShapeDtypeStruct(SPLIT(S), F32)],
                  scratch_shapes=[pltpu.VMEM((CHUNK, DIL_OUT), F32), pltpu.VMEM((CHUNK, DIL_OUT), F32)],
                  compiler_params=_params("arbitrary"))(qkv, qkv, qkv, qkv, qkv)


STAT_OFFSET = HEAD_DIM // 2


def _dil_bwd(qkv, stats, do, group, *, name):
    S = qkv.shape[1]
    dilation = DIL_PAIRS[group][1]
    bps = (S // dilation) // DIL_W
    slopes = _dil_slopes(group)
    nchunk = S // CHUNK
    nt = (((1,), (1,)), ((), ()))
    tn = (((0,), (0,)), ((), ()))

    def body(q_ref, k_ref, v_ref, kp_ref, vp_ref, ln_ref, don_ref, dqn_ref, dkn_ref, dvn_ref,
             dk_s, dv_s, l_ref, do_ref, dq_ref):
        step = pl.program_id(0)
        n = nchunk - 1 - step
        for i in range(4):
            rows = slice(i * DIL_W, (i + 1) * DIL_W)
            nat = _dil_rows(4 * n + i, S, dilation)
            for half in range(DIL_OUT // LANES):
                cols = slice(half * LANES, (half + 1) * LANES)
                l_ref[rows, cols] = ln_ref[half, nat, :]
                do_ref[rows, cols] = don_ref[half, nat, :]

        @pl.when(step == 0)
        def _():
            dk_s[:, CHUNK:] = jnp.zeros((DIL_OUT, DIL_W), F32)
            dv_s[:, CHUNK:] = jnp.zeros((DIL_OUT, DIL_W), F32)

        dk_s[:, :CHUNK] = jnp.zeros((DIL_OUT, CHUNK), F32)
        dv_s[:, :CHUNK] = jnp.zeros((DIL_OUT, CHUNK), F32)
        for i in range(4):
            valid, rel = _dil_tiles(i, n, bps)
            rows = slice(i * DIL_W, (i + 1) * DIL_W)
            window = slice(i * DIL_W, (i + 2) * DIL_W)
            for h in range(DIL_HEADS):
                cols = slice(h * HEAD_DIM, (h + 1) * HEAD_DIM)
                qh = q_ref[rows, cols]
                k2, v2 = _dil_window(k_ref, kp_ref, i, cols), _dil_window(v_ref, vp_ref, i, cols)
                lh = l_ref[rows, h * HEAD_DIM:h * HEAD_DIM + 1]
                shift = l_ref[rows, h * HEAD_DIM + STAT_OFFSET:h * HEAD_DIM + STAT_OFFSET + 1]
                s = lax.dot_general(qh, k2, nt, preferred_element_type=F32) * ATTN_SCALE - (slopes[h] * dilation) * rel
                p = jnp.exp(jnp.where(valid, s, NEG_INF) - lh)
                dob = do_ref[rows, cols].astype(_CD)
                ds = p * (lax.dot_general(dob, v2, nt, preferred_element_type=F32) + shift)
                dsb = (ds * ATTN_SCALE).astype(_CD)
                dq_ref[rows, cols] = jnp.dot(dsb, k2, preferred_element_type=F32)
                dk_s[cols, window] += lax.dot_general(qh, dsb, tn, preferred_element_type=F32)
                dv_s[cols, window] += lax.dot_general(dob, p.astype(_CD), tn, preferred_element_type=F32)
        for i in range(4):
            rows = slice(i * DIL_W, (i + 1) * DIL_W)
            done = slice((i + 1) * DIL_W, (i + 2) * DIL_W)
            nat = _dil_rows(4 * n + i, S, dilation)
            dkb, dvb = dk_s[:, done].T, dv_s[:, done].T
            for half in range(DIL_OUT // LANES):
                cols = slice(half * LANES, (half + 1) * LANES)
                dqn_ref[half, nat, :] = dq_ref[rows, cols]
                dkn_ref[half, nat, :] = dkb[:, cols]
                dvn_ref[half, nat, :] = dvb[:, cols]
        dk_s[:, CHUNK:] = dk_s[:, :DIL_W]
        dv_s[:, CHUNK:] = dv_s[:, :DIL_W]

    def cur(which):
        return pl.BlockSpec((None, CHUNK, DIL_OUT), lambda s: (which, nchunk - 1 - s, 0))

    def prev(which):
        return pl.BlockSpec((None, DIL_W, DIL_OUT), lambda s: (which, jnp.maximum(4 * (nchunk - 1 - s) - 1, 0), 0))

    whole = pl.BlockSpec(SPLIT(S), lambda s: (0, 0, 0))
    shp = jax.ShapeDtypeStruct(SPLIT(S), F32)
    tile = pltpu.VMEM((CHUNK, DIL_OUT), F32)
    return _pcall(body, name=name, grid=(nchunk,),
                  in_specs=[cur(0), cur(1), cur(2), prev(1), prev(2), whole, whole],
                  out_specs=[whole, whole, whole], out_shape=[shp, shp, shp],
                  scratch_shapes=[pltpu.VMEM((DIL_OUT, CHUNK + DIL_W), F32), pltpu.VMEM((DIL_OUT, CHUNK + DIL_W), F32),
                                  tile, tile, tile],
                  compiler_params=pltpu.CompilerParams(dimension_semantics=("arbitrary",),
                                                       vmem_limit_bytes=VMEM_LIMIT_RESIDENT))(
        qkv, qkv, qkv, qkv, qkv, stats, do)


def _dil_mix_fwd(os_, ls_, *, name, tm=512):
    nh, S, _ = os_[0].shape

    def body(o0, o1, o2, l0, l1, l2, out_ref):
        for half in range(nh):
            ls = [l0[half], l1[half], l2[half]]
            m = jnp.maximum(jnp.maximum(ls[0], ls[1]), ls[2])
            es = [jnp.exp(l - m) for l in ls]
            den = es[0] + es[1] + es[2]
            mixed = (es[0] * o0[half] + es[1] * o1[half] + es[2] * o2[half]) / den
            out_ref[:, half * LANES:(half + 1) * LANES] = mixed.astype(out_ref.dtype)

    halves = pl.BlockSpec((nh, tm, LANES), lambda i: (0, i, 0))
    row = pl.BlockSpec((tm, nh * LANES), lambda i: (i, 0))
    return _pcall(body, name=name, grid=(S // tm,), in_specs=[halves] * 6, out_specs=row,
                  out_shape=jax.ShapeDtypeStruct((S, nh * LANES), _CD), compiler_params=_params("parallel"))(*os_, *ls_)


def _dil_mix_bwd(doa, os_, ls_, *, name, tm=512, after=None):
    nh, S, _ = os_[0].shape

    def body(d_ref, o0, o1, o2, l0, l1, l2, do0, do1, do2, st0, st1, st2):
        first = lax.broadcasted_iota(jnp.int32, (tm, LANES), 1) % HEAD_DIM < STAT_OFFSET
        for half in range(nh):
            dv = d_ref[:, half * LANES:(half + 1) * LANES]
            ls = [l0[half], l1[half], l2[half]]
            m = jnp.maximum(jnp.maximum(ls[0], ls[1]), ls[2])
            es = [jnp.exp(l - m) for l in ls]
            den = es[0] + es[1] + es[2]
            al = [e / den for e in es]
            da = [_head_sum(dv * o[half]) for o in (o0, o1, o2)]
            mean = al[0] * da[0] + al[1] * da[1] + al[2] * da[2]
            for a, l, do_ref, st_ref in zip(al, ls, (do0, do1, do2), (st0, st1, st2)):
                do_ref[half] = a * dv
                st_ref[half] = jnp.where(first, l, -a * mean)

    halves = pl.BlockSpec((nh, tm, LANES), lambda i: (0, i, 0))
    row = pl.BlockSpec((tm, nh * LANES), lambda i: (i, 0))
    shp = jax.ShapeDtypeStruct((nh, S, LANES), F32)
    return _pcall(body, after, name=name, grid=(S // tm,), in_specs=[row] + [halves] * 6, out_specs=[halves] * 6,
                  out_shape=[shp] * 6, compiler_params=_params("parallel"))(doa, *os_, *ls_)


FOX_T = 512


PACK = 2 * HEAD_DIM
HEAD_PAIRS = N_FOX_HEADS // 2
FOX_HPS = 8
Q_BLOCK0 = 0
K_BLOCK0 = FOX_WIDTH // PACK
V_BLOCK0 = 2 * FOX_WIDTH // PACK


def _pieces(x):
    hi = x.astype(jnp.bfloat16).astype(F32)
    r = x - hi
    mid = r.astype(jnp.bfloat16).astype(F32)
    lo = (r - mid).astype(jnp.bfloat16).astype(F32)
    return [hi, mid, lo]


def _extras(first, second, rows):
    lane = lax.broadcasted_iota(jnp.int32, (rows, HEAD_DIM), 1)
    out = jnp.zeros((rows, HEAD_DIM), F32)
    for base, triple in ((0, first), (3, second)):
        if all(isinstance(v, float) for v in triple) and len(set(triple)) == 1:
            if triple[0] != 0.0:
                out = jnp.where(jnp.logical_and(lane >= base, lane < base + 3), triple[0], out)
        else:
            for idx, val in enumerate(triple):
                out = jnp.where(lane == base + idx, val, out)
    return out


def _head_column(c, h):
    lane = lax.broadcasted_iota(jnp.int32, c.shape, 1)
    return jnp.sum(jnp.where(lane == h, c, 0.0), axis=1, keepdims=True)


ONES3 = [1.0, 1.0, 1.0]
ZEROS3 = [0.0, 0.0, 0.0]


def _fox_pack_fwd(qkv, c, *, name, tm=1024):
    S = qkv.shape[0]

    def body(q_ref, k_ref, v_ref, c_ref, qo_ref, ko_ref, vo_ref):
        hp = pl.program_id(1)
        cv = c_ref[...]
        v_extras = jnp.where(lax.broadcasted_iota(jnp.int32, (tm, HEAD_DIM), 1) < 3, 1.0, 0.0).astype(vo_ref.dtype)
        for hh in range(2):
            ch = _pieces(_head_column(cv, 2 * hp + hh))
            src = slice(hh * HEAD_DIM, (hh + 1) * HEAD_DIM)
            lo = slice(hh * PACK, hh * PACK + HEAD_DIM)
            hi = slice(hh * PACK + HEAD_DIM, (hh + 1) * PACK)
            qo_ref[:, lo] = (q_ref[:, src].astype(F32) * ATTN_SCALE).astype(qo_ref.dtype)
            qo_ref[:, hi] = _extras(ch, ONES3, tm).astype(qo_ref.dtype)
            ko_ref[:, lo] = k_ref[:, src]
            ko_ref[:, hi] = _extras(ONES3, [-p for p in ch], tm).astype(ko_ref.dtype)
            vo_ref[:, lo] = v_ref[:, src]
            vo_ref[:, hi] = v_extras

    def src(block0):
        return pl.BlockSpec((tm, PACK), lambda i, hp: (i, block0 + hp))

    out = pl.BlockSpec((tm, 2 * PACK), lambda i, hp: (i, hp))
    shp = jax.ShapeDtypeStruct((S, N_FOX_HEADS * PACK), _CD)
    return _pcall(body, name=name, grid=(S // tm, HEAD_PAIRS),
                  in_specs=[src(Q_BLOCK0), src(K_BLOCK0), src(V_BLOCK0), pl.BlockSpec((tm, PACK), lambda i, hp: (i, 0))],
                  out_specs=[out, out, out], out_shape=[shp, shp, shp],
                  compiler_params=_params("parallel", "parallel"))(qkv, qkv, qkv, c)


def _fox_fwd(qp, kp, vp, *, name):
    S = qp.shape[0]
    nt = S // FOX_T
    nt_dims = (((1,), (1,)), ((), ()))
    tn_dims = (((0,), (0,)), ((), ()))

    def body(i_tab, j_tab, q_ref, k_ref, v_ref, o_ref, l_ref, m_s, acc_s):
        t = pl.program_id(1)
        i, j = i_tab[t], j_tab[t]

        @pl.when(j == 0)
        def _():
            m_s[...] = jnp.full((FOX_HPS, 1, FOX_T), NEG_INF, F32)
            acc_s[...] = jnp.zeros((FOX_HPS, PACK, FOX_T), F32)

        def tile(diagonal):
            for hh in range(FOX_HPS):
                cols = slice(hh * PACK, (hh + 1) * PACK)
                st = lax.dot_general(k_ref[:, cols], q_ref[:, cols], nt_dims, preferred_element_type=F32)
                if diagonal:
                    key = lax.broadcasted_iota(jnp.int32, (FOX_T, FOX_T), 0)
                    qry = lax.broadcasted_iota(jnp.int32, (FOX_T, FOX_T), 1)
                    st = jnp.where(key <= qry, st, NEG_INF)
                m_old = m_s[hh]
                m_new = jnp.maximum(m_old, jnp.max(st, axis=0, keepdims=True))
                pt = jnp.exp(st - m_new)
                acc_s[hh] = jnp.exp(m_old - m_new) * acc_s[hh] + lax.dot_general(
                    v_ref[:, cols], pt.astype(_CD), tn_dims, preferred_element_type=F32)
                m_s[hh] = m_new

        @pl.when(j < i)
        def _():
            tile(False)

        @pl.when(j == i)
        def _():
            tile(True)
            for hh in range(FOX_HPS):
                acc = acc_s[hh]
                den = acc[HEAD_DIM:HEAD_DIM + 1, :]
                cols = slice(hh * HEAD_DIM, (hh + 1) * HEAD_DIM)
                o_ref[:, cols] = (acc[:HEAD_DIM, :] / den).T
                l_ref[:, cols] = jnp.broadcast_to(m_s[hh] + jnp.log(den), (HEAD_DIM, FOX_T)).T

    pairs = [(i, j) for i in range(nt) for j in range(i + 1)]
    i_tab = jnp.asarray([p[0] for p in pairs], jnp.int32)
    j_tab = jnp.asarray([p[1] for p in pairs], jnp.int32)
    qs = pl.BlockSpec((FOX_T, FOX_HPS * PACK), lambda hp, t, it, jt: (it[t], hp))
    ks = pl.BlockSpec((FOX_T, FOX_HPS * PACK), lambda hp, t, it, jt: (jt[t], hp))
    os_ = pl.BlockSpec((FOX_T, FOX_HPS * HEAD_DIM), lambda hp, t, it, jt: (it[t], hp))
    shp = jax.ShapeDtypeStruct((S, FOX_WIDTH), F32)
    grid_spec = pltpu.PrefetchScalarGridSpec(
        num_scalar_prefetch=2, grid=(N_FOX_HEADS // FOX_HPS, len(pairs)), in_specs=[qs, ks, ks], out_specs=[os_, os_],
        scratch_shapes=[pltpu.VMEM((FOX_HPS, 1, FOX_T), F32), pltpu.VMEM((FOX_HPS, PACK, FOX_T), F32)])
    return _pcall(body, name=name, grid_spec=grid_spec, out_shape=[shp, shp],
                  compiler_params=_params("parallel", "arbitrary"))(i_tab, j_tab, qp, kp, vp)


def _fox_pack_bwd(qkv, c, o, lse, do, *, name, tm=1024, after=None):
    S = qkv.shape[0]

    def body(q_ref, c_ref, o_ref, l_ref, do_ref, qo_ref, do_out_ref):
        hp = pl.program_id(1)
        cv = c_ref[...]
        for hh in range(2):
            src = slice(hh * HEAD_DIM, (hh + 1) * HEAD_DIM)
            lo = slice(hh * PACK, hh * PACK + HEAD_DIM)
            hi = slice(hh * PACK + HEAD_DIM, (hh + 1) * PACK)
            shift = _head_column(cv, 2 * hp + hh) - l_ref[:, hh * HEAD_DIM:hh * HEAD_DIM + 1]
            dov = do_ref[:, src]
            dsum = jnp.sum(dov * o_ref[:, src], axis=-1, keepdims=True)
            qo_ref[:, lo] = (q_ref[:, src].astype(F32) * ATTN_SCALE).astype(qo_ref.dtype)
            qo_ref[:, hi] = _extras(_pieces(shift), ONES3, tm).astype(qo_ref.dtype)
            do_out_ref[:, lo] = dov.astype(do_out_ref.dtype)
            do_out_ref[:, hi] = _extras(_pieces(-dsum), ZEROS3, tm).astype(do_out_ref.dtype)

    pair = pl.BlockSpec((tm, PACK), lambda i, hp: (i, hp))
    out = pl.BlockSpec((tm, 2 * PACK), lambda i, hp: (i, hp))
    shp = jax.ShapeDtypeStruct((S, N_FOX_HEADS * PACK), _CD)
    return _pcall(body, after, name=name, grid=(S // tm, HEAD_PAIRS),
                  in_specs=[pl.BlockSpec((tm, PACK), lambda i, hp: (i, Q_BLOCK0 + hp)),
                            pl.BlockSpec((tm, PACK), lambda i, hp: (i, 0)), pair, pair, pair],
                  out_specs=[out, out], out_shape=[shp, shp],
                  compiler_params=_params("parallel", "parallel"))(qkv, c, o, lse, do)


def _fox_bwd(qp, kp, vp, dop, *, name):
    S = qp.shape[0]
    nt = S // FOX_T
    nt_dims = (((1,), (1,)), ((), ()))
    tn_dims = (((0,), (0,)), ((), ()))

    def body(i_tab, j_tab, q_ref, k_ref, v_ref, do_ref, dq_ref, dk_ref, dv_ref, dc_ref, dr_ref,
             dq_s, dk_s, dv_s, dc_s, dr_s):
        t = pl.program_id(1)
        i, j = i_tab[t], j_tab[t]

        @pl.when(t == 0)
        def _():
            dq_s[...] = jnp.zeros((S, FOX_HPS * PACK), F32)
            dr_s[...] = jnp.zeros((FOX_HPS, 1, S), F32)

        @pl.when(i == j)
        def _():
            dk_s[...] = jnp.zeros((FOX_T, FOX_HPS * PACK), F32)
            dv_s[...] = jnp.zeros((FOX_T, FOX_HPS * PACK), F32)
            dc_s[...] = jnp.zeros((FOX_HPS, FOX_T, 1), F32)

        def tile(diagonal):
            rows = pl.ds(pl.multiple_of(i * FOX_T, FOX_T), FOX_T)
            for hh in range(FOX_HPS):
                cols = slice(hh * PACK, (hh + 1) * PACK)
                qv, kv, vv, dov = q_ref[:, cols], k_ref[:, cols], v_ref[:, cols], do_ref[:, cols]
                pt = jnp.exp(lax.dot_general(kv, qv, nt_dims, preferred_element_type=F32))
                if diagonal:
                    key = lax.broadcasted_iota(jnp.int32, (FOX_T, FOX_T), 0)
                    qry = lax.broadcasted_iota(jnp.int32, (FOX_T, FOX_T), 1)
                    pt = jnp.where(key <= qry, pt, 0.0)
                dst = pt * lax.dot_general(vv, dov, nt_dims, preferred_element_type=F32)
                dsb = dst.astype(_CD)
                dc_s[hh] += jnp.sum(dst, axis=1, keepdims=True)
                dr_s[hh, :, rows] += jnp.sum(dst, axis=0, keepdims=True)
                dv_s[:, cols] += jnp.dot(pt.astype(_CD), dov, preferred_element_type=F32)
                dk_s[:, cols] += jnp.dot(dsb, qv, preferred_element_type=F32)
                dq_s[rows, cols] += lax.dot_general(dsb, kv, tn_dims, preferred_element_type=F32)

        @pl.when(i > j)
        def _():
            tile(False)

        @pl.when(i == j)
        def _():
            tile(True)

        @pl.when(i == nt - 1)
        def _():
            for hh in range(FOX_HPS):
                src = slice(hh * PACK, hh * PACK + HEAD_DIM)
                dst_cols = slice(hh * HEAD_DIM, (hh + 1) * HEAD_DIM)
                dk_ref[:, dst_cols] = dk_s[:, src].astype(dk_ref.dtype)
                dv_ref[:, dst_cols] = dv_s[:, src].astype(dv_ref.dtype)
                dc_ref[:, dst_cols] = jnp.broadcast_to(dc_s[hh], (FOX_T, HEAD_DIM))

        @pl.when(t == len(pairs) - 1)
        def _():
            for hh in range(FOX_HPS):
                dq_ref[:, hh * HEAD_DIM:(hh + 1) * HEAD_DIM] = (
                    dq_s[:, hh * PACK:hh * PACK + HEAD_DIM] * ATTN_SCALE).astype(dq_ref.dtype)
            dr_ref[...] = dr_s[...]

    pairs = [(i, j) for j in range(nt) for i in range(j, nt)]
    i_tab = jnp.asarray([p[0] for p in pairs], jnp.int32)
    j_tab = jnp.asarray([p[1] for p in pairs], jnp.int32)
    wide, narrow = FOX_HPS * PACK, FOX_HPS * HEAD_DIM
    qs = pl.BlockSpec((FOX_T, wide), lambda hp, t, it, jt: (it[t], hp))
    ks = pl.BlockSpec((FOX_T, wide), lambda hp, t, it, jt: (jt[t], hp))
    whole = pl.BlockSpec((S, narrow), lambda hp, t, it, jt: (0, hp))
    cs = pl.BlockSpec((FOX_T, narrow), lambda hp, t, it, jt: (jt[t], hp))
    rs = pl.BlockSpec((FOX_HPS, 1, S), lambda hp, t, it, jt: (hp, 0, 0))
    shp = jax.ShapeDtypeStruct((S, FOX_WIDTH), _CD)
    grid_spec = pltpu.PrefetchScalarGridSpec(
        num_scalar_prefetch=2, grid=(N_FOX_HEADS // FOX_HPS, len(pairs)), in_specs=[qs, ks, ks, qs],
        out_specs=[whole, cs, cs, cs, rs],
        scratch_shapes=[pltpu.VMEM((S, wide), F32), pltpu.VMEM((FOX_T, wide), F32),
                        pltpu.VMEM((FOX_T, wide), F32), pltpu.VMEM((FOX_HPS, FOX_T, 1), F32),
                        pltpu.VMEM((FOX_HPS, 1, S), F32)])
    return _pcall(body, name=name, grid_spec=grid_spec,
                  out_shape=[shp, shp, shp, jax.ShapeDtypeStruct((S, FOX_WIDTH), F32),
                             jax.ShapeDtypeStruct((N_FOX_HEADS, 1, S), F32)],
                  compiler_params=_params("parallel", "arbitrary"))(i_tab, j_tab, qp, kp, vp, dop)


def _layer_step(x, tgt, w, p, late_weights=None, grad_sink=None, after=None, first_weights=None):
    S = x.shape[0]
    after_norm, after_proj = after if after is not None else (None, None)
    h = _rms_fwd(x, p["norm_mix_g"], name="rms_mix", after=after_norm)
    if first_weights is not None:
        w = {**w, **first_weights(h)}
    qkv = _mm(h, w["qkv"][:, 3 * DIL_WIDTH:], name="proj_fox", out_dtype=_CD, tn=768, tm=2048, after=after_proj)
    dil_qkv = _proj_dil(h, w["qkv"], name="proj_dil")
    zf = _mm(h, w["f"], name="proj_f")
    gl = _mm(h, w["g"], name="proj_gate", tn=1024, out_dtype=_CD)

    dil_o, dil_l = [], []
    for g in range(N_DIL_GROUPS):
        og, lg = _dil_fwd(dil_qkv[g], g, name=f"dil_fwd{g}")
        dil_o.append(og), dil_l.append(lg)
    o_a = _dil_mix_fwd(dil_o, dil_l, name="dil_mix")

    c = _fox_cumsum(zf, p["b_fgt"], name="fox_cumsum")
    fqp, fkp, fvp = _fox_pack_fwd(qkv, c, name="fox_pack")
    o_b, flse = _fox_fwd(fqp, fkp, fvp, name="fox_fwd")

    if late_weights is not None:
        w = {**w, **late_weights(o_b)}
    y_a = _mm(o_a, w["dil_out"], name="y_a", tn=1024, out_dtype=_CD)
    y_b = _mm(o_b, w["fox_out"], name="y_b", tn=1024, out_dtype=_CD)
    merged, x1, h2 = _gated_mix_out(gl, p["b_gate"], y_a, y_b, w["out"], x, p["norm_ffn_g"], name="mix_out")
    gate, up, act = _ffn_in_act(h2, w["ffn_in"], name="ffn_in")
    loss, dx2, dg_final = _ffn_down_loss(act, w["ffn_down"], x1, p["norm_final_g"], tgt, name="ffn_down_loss")

    gw_ffn_down = _mm(act, dx2, name="gw_ffn_down", ta=True, out_dtype=_CD, tm=1408)
    dgu = _d_swiglu(dx2, w["ffn_down"], gate, up, name="d_swiglu")
    gw_ffn_in = _mm(h2, dgu, name="gw_ffn_in", ta=True, out_dtype=_CD, tn=1408, out_blocks=1408, b_halves=True)
    sink = grad_sink if grad_sink is not None else (lambda group, grads: None)
    tok = sink("ffn", dict(ffn_in=gw_ffn_in, ffn_down=gw_ffn_down))
    dx1, dg_ffn = _d_h2(dgu, w["ffn_in"], x1, p["norm_ffn_g"], dx2, name="d_h2", after=tok)

    gw_out = _mm(merged, dx1, name="gw_out", ta=True, out_dtype=_CD)
    dy_a, dy_b, dgl, db_gate = _gate_bwd(dx1, w["out"], gl, p["b_gate"], y_a, y_b, name="gate_bwd")
    do_a = _mm(dy_a, w["dil_out"], name="d_o_a", tb=True)
    gw_dil_out = _mm(o_a, dy_a, name="gw_dil_out", ta=True, out_dtype=_CD, tn=1024)
    do_b = _mm(dy_b, w["fox_out"], name="d_o_b", tb=True)
    gw_fox_out = _mm(o_b, dy_b, name="gw_fox_out", ta=True, out_dtype=_CD, tn=1024)
    tok = sink("mix", dict(dil_out=gw_dil_out, fox_out=gw_fox_out, out=gw_out))

    bqp, bdop = _fox_pack_bwd(qkv, c, o_b, flse, do_b, name="fox_pack_bwd", after=tok)
    dqp, dkp, dvp, dck, dcq = _fox_bwd(bqp, fkp, fvp, bdop, name="fox_bwd")
    dc = dcq[:, 0, :].T - dck.reshape(S, N_FOX_HEADS, HEAD_DIM)[:, :, 0]
    dc = jnp.pad(dc, ((0, 0), (0, F_PAD - N_FOX_HEADS)))
    dzf, db_fgt = _fox_cumsum_bwd(dc, zf, p["b_fgt"], name="fox_cumsum_bwd")

    douts = _dil_mix_bwd(do_a, dil_o, dil_l, name="dil_mix_bwd", after=tok)
    dqs, dks, dvs = [], [], []
    for g in range(N_DIL_GROUPS):
        dq, dk, dv = _dil_bwd(dil_qkv[g], douts[3 + g], douts[g], g, name=f"dil_bwd{g}")
        for parts, t in ((dqs, dq), (dks, dk), (dvs, dv)):
            parts.extend([t[0].astype(_CD), t[1].astype(_CD)])
    dqkv = jnp.concatenate(dqs + dks + dvs + [dqp, dkp, dvp], axis=1)

    gw_qkv = _mm(h, dqkv, name="gw_qkv", ta=True, out_dtype=_CD, tn=768, tk=S)
    gw_g = _mm(h, dgl, name="gw_gate", ta=True, out_dtype=_CD, tk=S)
    gw_f = _mm(h, dzf, name="gw_f", ta=True, out_dtype=_CD)
    tok = sink("in", dict(qkv=gw_qkv, f=gw_f, g=gw_g))
    dx, dg_mix = _mm(dqkv, w["qkv"], name="d_h", tb=True, tk=QKV_COLS, tm=256, more=((dgl, w["g"]), (dzf, w["f"])),
                     rms_bwd=(x, p["norm_mix_g"], dx1), after=tok)

    gw = dict(qkv=gw_qkv, f=gw_f, g=gw_g, dil_out=gw_dil_out, fox_out=gw_fox_out, out=gw_out, ffn_in=gw_ffn_in,
              ffn_down=gw_ffn_down)
    small = dict(norm_mix_g=dg_mix, b_fgt=db_fgt, b_gate=db_gate, norm_ffn_g=dg_ffn, norm_final_g=dg_final)
    return loss, dx, gw, small


def _position():
    return lax.axis_index("x"), lax.axis_index("y"), lax.axis_index("c")


def _other_chips(x, y):
    return [(1 - x, y), (x, 1 - y), (1 - x, 1 - y)]


ROW_TILE = 16


def _row_chunks(rows, want=4):
    n = want
    while n > 1 and rows % (n * ROW_TILE):
        n //= 2
    return n


SEM_SPEC = pl.BlockSpec(memory_space=pltpu.SEMAPHORE)
ANY_SPEC = pl.BlockSpec(memory_space=pl.ANY)
DATAFLOW = pltpu.SideEffectType.DATAFLOW_SIDE_EFFECTING


def _in_hbm(a):
    return pltpu.with_memory_space_constraint(a, pltpu.HBM)


def _split_copy_start(srcs, land_shapes, copies, after, *, name):
    n, m = len(srcs), len(land_shapes)

    def body(*refs):
        src_refs, land_refs = refs[:n], refs[n:n + m]
        send_sems, recv_sems = refs[n + m + 1], refs[n + m + 2]
        token = refs[-1]
        x, y, c = _position()
        for k, (src, dst, peer) in enumerate(copies(x, y, c, src_refs, land_refs)):
            pltpu.make_async_remote_copy(src_ref=src, dst_ref=dst, send_sem=send_sems.at[k], recv_sem=recv_sems.at[k],
                                         device_id=peer, device_id_type=MESH).start()
        token[...] = jnp.zeros_like(token)

    lands = [lax.empty(s.shape, s.dtype) for s in land_shapes]
    count = len(copies(0, 0, 0, srcs, lands))
    out = _pcall(
        body, name=name,
        out_shape=(pltpu.SemaphoreType.DMA((count,)), pltpu.SemaphoreType.DMA((count,)),
                   *[pltpu.HBM(s.shape, s.dtype) for s in srcs], *[pltpu.HBM(s.shape, s.dtype) for s in land_shapes],
                   jax.ShapeDtypeStruct((8, 128), F32)),
        in_specs=[HBM_SPEC] * (n + m) + [ANY_SPEC],
        out_specs=(SEM_SPEC, SEM_SPEC, *[HBM_SPEC] * (n + m), pl.BlockSpec(memory_space=pltpu.VMEM)),
        input_output_aliases={k: 2 + k for k in range(n + m)},
        compiler_params=pltpu.CompilerParams(has_side_effects=DATAFLOW),
    )(*[_in_hbm(s) for s in srcs], *[_in_hbm(l) for l in lands], after)
    return out[0], out[1], list(out[2:2 + n]), list(out[2 + n:2 + n + m]), out[-1]


def _split_copy_wait(send_sems, recv_sems, srcs, lands, copies, after, *, name):
    n, m = len(srcs), len(lands)

    def body(*refs):
        src_refs, land_refs = refs[:n], refs[n:n + m]
        send, recv = refs[n + m], refs[n + m + 1]
        x, y, c = _position()
        for k, (src, dst, peer) in enumerate(copies(x, y, c, src_refs, land_refs)):
            cp = pltpu.make_async_remote_copy(src_ref=src, dst_ref=dst, send_sem=send.at[k], recv_sem=recv.at[k],
                                              device_id=peer, device_id_type=MESH)
            cp.wait_send()
            cp.wait_recv()

    afters = list(after) if isinstance(after, (list, tuple)) else [after]
    out = _pcall(
        body, name=name,
        out_shape=tuple(pltpu.HBM(s.shape, s.dtype) for s in list(srcs) + list(lands)),
        in_specs=[HBM_SPEC] * (n + m) + [SEM_SPEC, SEM_SPEC] + [ANY_SPEC] * len(afters),
        out_specs=tuple([HBM_SPEC] * (n + m)),
        input_output_aliases={k: k for k in range(n + m)},
        compiler_params=pltpu.CompilerParams(has_side_effects=DATAFLOW),
    )(*srcs, *lands, send_sems, recv_sems, *afters)
    return list(out[:n]), list(out[n:])


def _gather_copies(x, y, c, shard_refs, land_refs):
    out = []
    for s, l in zip(shard_refs, land_refs):
        half = s.shape[0] // 2
        nq = _row_chunks(half)
        for cx, cy in _other_chips(x, y):
            for q in range(nq):
                rows = pl.ds(c * half + q * (half // nq), half // nq)
                out.append((s.at[rows, :], l.at[2 * x + y, rows, :], (cx, cy, c)))
    return out


def _gather_whole_copies(x, y, c, shard_refs, land_refs):
    out = []
    for s, l in zip(shard_refs, land_refs):
        nq = _row_chunks(s.shape[0])
        for cx, cy in _other_chips(x, y):
            for q in range(nq):
                rows = pl.ds(q * (s.shape[0] // nq), s.shape[0] // nq)
                out.append((s.at[rows, :], l.at[2 * x + y, rows, :], (cx, cy, c)))
    return out


def _scatter_all_copies(x, y, c, block_refs, land_refs):
    out = []
    for g, l in zip(block_refs, land_refs):
        half = g.shape[1] // 2
        nq = _row_chunks(half)
        size = half // nq
        for q in range(nq):
            rows = pl.ds((1 - c) * half + q * size, size)
            out.append((g.at[2 * x + y, rows, :], l.at[0, pl.ds(q * size, size), :], (x, y, 1 - c)))
        for r, (cx, cy) in enumerate(_other_chips(x, y)):
            for j in range(2):
                h = c if j == 0 else 1 - c
                for q in range(nq):
                    rows = pl.ds(h * half + q * size, size)
                    out.append((g.at[2 * cx + cy, rows, :], l.at[1 + 2 * r + j, pl.ds(q * size, size), :], (cx, cy, h)))
    return out


def _forward_halves(lands, *, name):
    n = len(lands)

    def body(*refs):
        ins = refs[:n]
        send_sems, recv_sems = refs[2 * n:]
        x, y, c = _position()
        copies = []
        for w in range(n):
            half = ins[w].shape[1] // 2
            for r, (cx, cy) in enumerate(_other_chips(x, y)):
                blk = ins[w].at[2 * cx + cy, pl.ds(c * half, half), :]
                cp = pltpu.make_async_remote_copy(src_ref=blk, dst_ref=blk, send_sem=send_sems.at[w, r],
                                                  recv_sem=recv_sems.at[w, r], device_id=(x, y, 1 - c),
                                                  device_id_type=MESH)
                cp.start()
                copies.append(cp)
        for w in range(n):
            half = ins[w].shape[1] // 2
            for r, (cx, cy) in enumerate(_other_chips(x, y)):
                blk = ins[w].at[2 * cx + cy, pl.ds((1 - c) * half, half), :]
                pltpu.make_async_remote_copy(src_ref=blk, dst_ref=blk, send_sem=send_sems.at[w, r],
                                             recv_sem=recv_sems.at[w, r], device_id=(x, y, 1 - c),
                                             device_id_type=MESH).wait_recv()
        for cp in copies:
            cp.wait_send()

    return _pcall(
        body, name=name, in_specs=[HBM_SPEC] * n, out_specs=[HBM_SPEC] * n,
        out_shape=[jax.ShapeDtypeStruct(l.shape, l.dtype) for l in lands],
        input_output_aliases={k: k for k in range(n)},
        scratch_shapes=[pltpu.SemaphoreType.DMA((n, 3)), pltpu.SemaphoreType.DMA((n, 3))],
    )(*lands)


def _share_halves(halves, *, name):
    n = len(halves)

    def body(*refs):
        ins, outs = refs[:n], refs[n:2 * n]
        send_sems, recv_sems = refs[2 * n:]
        x, y, c = _position()
        copies = []
        for w in range(n):
            cp = pltpu.make_async_remote_copy(src_ref=ins[w], dst_ref=outs[w], send_sem=send_sems.at[w],
                                              recv_sem=recv_sems.at[w], device_id=(x, y, 1 - c), device_id_type=MESH)
            cp.start()
            copies.append(cp)
        for cp in copies:
            cp.wait()

    return _pcall(
        body, name=name, in_specs=[HBM_SPEC] * n, out_specs=[HBM_SPEC] * n,
        out_shape=[jax.ShapeDtypeStruct(h.shape, h.dtype) for h in halves],
        scratch_shapes=[pltpu.SemaphoreType.DMA((n,)), pltpu.SemaphoreType.DMA((n,))],
    )(*halves)


def _sum_small(part, after=None):
    rows, width = part.shape

    def body(x_ref, out_ref, all_ref, send_sems, recv_sems):
        x, y, c = _position()
        me, sibling = (x, y, c), (x, y, 1 - c)
        chips = _other_chips(x, y)

        def block(px, py, pc):
            return all_ref.at[pl.ds((4 * px + 2 * py + pc) * rows, rows), :]

        def copy(k, blk, to, src=None):
            return pltpu.make_async_remote_copy(
                src_ref=block(*blk) if src is None else src, dst_ref=block(*blk), send_sem=send_sems.at[k],
                recv_sem=recv_sems.at[k], device_id=to, device_id_type=MESH)

        all_ref[pl.ds((4 * x + 2 * y + c) * rows, rows), :] = x_ref[...]
        first = [copy(0, me, sibling, src=x_ref)]
        first += [copy(1 + j, me, (*chip, c), src=x_ref) for j, chip in enumerate(chips)]
        for cp in first:
            cp.start()
        passed = [copy(4 + j, (*chip, c), sibling) for j, chip in enumerate(chips)]
        for j, chip in enumerate(chips):
            copy(1 + j, (*chip, c), me).wait_recv()
            passed[j].start()
        copy(0, sibling, me).wait_recv()
        for j, chip in enumerate(chips):
            copy(4 + j, (*chip, 1 - c), me).wait_recv()
        for cp in first + passed:
            cp.wait_send()
        total = all_ref[0:rows, :]
        for d in range(1, 8):
            total = total + all_ref[d * rows:(d + 1) * rows, :]
        out_ref[...] = total

    vm = pl.BlockSpec(memory_space=pltpu.VMEM)
    return _pcall(
        body, after, name="sum_small", in_specs=[vm], out_specs=vm, out_shape=jax.ShapeDtypeStruct((rows, width), F32),
        scratch_shapes=[pltpu.VMEM((8 * rows, width), F32), pltpu.SemaphoreType.DMA((7,)), pltpu.SemaphoreType.DMA((7,))],
    )(part)


def _row_tile(R, C, itemsize=4, budget=1 << 20):
    for t in (512, 256, 128, 64, 32, 16, 8):
        if R % t == 0 and t * C * itemsize <= budget:
            return t
    return R


def _add_all(g, recv, where, *, name):
    _, R, C = g.shape
    half = R // 2
    t = _row_tile(half, C)
    nb = half // t

    def body(w_ref, g_ref, r_ref, o_ref):
        total = g_ref[0].astype(F32)
        for k in range(7):
            total = total + r_ref[k].astype(F32)
        o_ref[...] = total

    grid_spec = pltpu.PrefetchScalarGridSpec(
        num_scalar_prefetch=1, grid=(nb,),
        in_specs=[pl.BlockSpec((1, t, C), lambda i, wr: (wr[0], wr[1] * nb + i, 0)),
                  pl.BlockSpec((7, t, C), lambda i, wr: (0, i, 0))],
        out_specs=pl.BlockSpec((t, C), lambda i, wr: (i, 0)))
    return _pcall(body, name=name, grid_spec=grid_spec, out_shape=jax.ShapeDtypeStruct((half, C), F32),
                  compiler_params=_params("parallel"))(where, g, recv)


def _adamw(w, g, m, v, *, name):
    R, C = w.shape
    t = _row_tile(R, C)
    c1 = 1.0 - ADAM_B1 ** ADAM_STEP
    c2 = 1.0 - ADAM_B2 ** ADAM_STEP

    def body(w_ref, g_ref, m_ref, v_ref, d_ref, nm_ref, nv_ref):
        gv = g_ref[...]
        mn = ADAM_B1 * m_ref[...] + (1.0 - ADAM_B1) * gv
        vn = ADAM_B2 * v_ref[...] + (1.0 - ADAM_B2) * (gv * gv)
        d_ref[...] = -ADAM_LR * ((mn / c1) / (jnp.sqrt(vn / c2) + ADAM_EPS) + ADAM_WD * w_ref[...])
        nm_ref[...] = mn
        nv_ref[...] = vn

    blk = pl.BlockSpec((t, C), lambda i: (i, 0))
    shp = jax.ShapeDtypeStruct((R, C), F32)
    return _pcall(body, name=name, grid=(R // t,), in_specs=[blk] * 4, out_specs=[blk] * 3, out_shape=[shp] * 3,
                  compiler_params=_params("parallel"))(w, g, m, v)


def _adamw_halves(w, mine, theirs, m, v, core, *, name, after=None):
    R, C = w.shape
    half = R // 2
    t = _row_tile(half, C)
    nbh = half // t
    c1 = 1.0 - ADAM_B1 ** ADAM_STEP
    c2 = 1.0 - ADAM_B2 ** ADAM_STEP

    def body(core_ref, w_ref, a_ref, b_ref, m_ref, v_ref, *rest):
        g_ref, d_ref, nm_ref, nv_ref = rest[-4:]
        gv = jnp.where(pl.program_id(0) // nbh == core_ref[0], a_ref[...], b_ref[...])
        mn = ADAM_B1 * m_ref[...] + (1.0 - ADAM_B1) * gv
        vn = ADAM_B2 * v_ref[...] + (1.0 - ADAM_B2) * (gv * gv)
        g_ref[...] = gv
        d_ref[...] = -ADAM_LR * ((mn / c1) / (jnp.sqrt(vn / c2) + ADAM_EPS) + ADAM_WD * w_ref[...])
        nm_ref[...] = mn
        nv_ref[...] = vn

    blk = pl.BlockSpec((t, C), lambda i, cr: (i, 0))
    hblk = pl.BlockSpec((t, C), lambda i, cr: (i % nbh, 0))
    shp = jax.ShapeDtypeStruct((R, C), F32)
    tied = [] if after is None else [after]
    grid_spec = pltpu.PrefetchScalarGridSpec(num_scalar_prefetch=1, grid=(2 * nbh,),
                                             in_specs=[blk, hblk, hblk, blk, blk] + [ANY_SPEC] * len(tied),
                                             out_specs=[blk] * 4)
    return _pcall(body, name=name, grid_spec=grid_spec, out_shape=[shp] * 4,
                  compiler_params=_params("parallel"))(core, w, mine, theirs, m, v, *tied)


BIG = ("w_in", "w_dil_out", "w_fox_out", "w_out", "w_ffn_in", "w_ffn_down")
SMALL = ("norm_mix_g", "b_fgt", "b_gate", "norm_ffn_g", "norm_final_g")
ORDER = ("norm_mix_g", "w_in", "b_fgt", "b_gate", "w_dil_out", "w_fox_out", "w_out", "norm_ffn_g", "w_ffn_in",
         "w_ffn_down", "norm_final_g")
SMALL_ROWS = {"norm_mix_g": (0, 1), "b_gate": (1, 3), "norm_ffn_g": (3, 4), "norm_final_g": (4, 5), "b_fgt": (5, 6)}


def _columns_to_blocks(full, ncol):
    K = full.shape[0]
    return full.reshape(K, 4, ncol).transpose(1, 0, 2)


def _pieces_to_blocks(pieces, ncol):
    spans, start = [], 0
    for piece in pieces:
        spans.append((piece, start, start + piece.shape[1]))
        start += piece.shape[1]
    assert start == 4 * ncol
    blocks = []
    for k in range(4):
        lo, hi = k * ncol, (k + 1) * ncol
        parts = [p[:, max(lo, a) - a:min(hi, b) - a] for p, a, b in spans if a < hi and b > lo]
        blocks.append(parts[0] if len(parts) == 1 else jnp.concatenate(parts, axis=1))
    return jnp.stack(blocks)


def _blocks_to_pieces(blocks, widths):
    n, K, ncol = blocks.shape
    assert sum(widths) == n * ncol
    pieces, lo = [], 0
    for width in widths:
        hi = lo + width
        parts = [blocks[k][:, max(lo, k * ncol) - k * ncol:min(hi, (k + 1) * ncol) - k * ncol]
                 for k in range(n) if k * ncol < hi and (k + 1) * ncol > lo]
        pieces.append(parts[0] if len(parts) == 1 else jnp.concatenate(parts, axis=1))
        lo = hi
    return pieces


def _blocks_to_columns(blocks):
    n, K, ncol = blocks.shape
    return blocks.transpose(1, 0, 2).reshape(K, n * ncol)


def kernel(x, norm_mix_g, w_in, b_fgt, b_gate, w_dil_out, w_fox_out, w_out, norm_ffn_g, w_ffn_in, w_ffn_down, norm_final_g, loss_target, m_norm_mix_g, m_w_in, m_b_fgt, m_b_gate, m_w_dil_out, m_w_fox_out, m_w_out, m_norm_ffn_g, m_w_ffn_in, m_w_ffn_down, m_norm_final_g, v_norm_mix_g, v_w_in, v_b_fgt, v_b_gate, v_w_dil_out, v_w_fox_out, v_w_out, v_norm_ffn_g, v_w_ffn_in, v_w_ffn_down, v_norm_final_g):
    weights = dict(norm_mix_g=norm_mix_g, w_in=w_in, b_fgt=b_fgt, b_gate=b_gate, w_dil_out=w_dil_out,
                   w_fox_out=w_fox_out, w_out=w_out, norm_ffn_g=norm_ffn_g, w_ffn_in=w_ffn_in, w_ffn_down=w_ffn_down,
                   norm_final_g=norm_final_g)
    m_in = dict(norm_mix_g=m_norm_mix_g, w_in=m_w_in, b_fgt=m_b_fgt, b_gate=m_b_gate, w_dil_out=m_w_dil_out,
                w_fox_out=m_w_fox_out, w_out=m_w_out, norm_ffn_g=m_norm_ffn_g, w_ffn_in=m_w_ffn_in,
                w_ffn_down=m_w_ffn_down, norm_final_g=m_norm_final_g)
    v_in = dict(norm_mix_g=v_norm_mix_g, w_in=v_w_in, b_fgt=v_b_fgt, b_gate=v_b_gate, w_dil_out=v_w_dil_out,
                w_fox_out=v_w_fox_out, w_out=v_w_out, norm_ffn_g=v_norm_ffn_g, w_ffn_in=v_w_ffn_in,
                w_ffn_down=v_w_ffn_down, norm_final_g=v_norm_final_g)
    c = lax.axis_index("c")
    chip = 2 * lax.axis_index("x") + lax.axis_index("y")

    shards = {n: weights[n][0].astype(_CD) for n in BIG}
    in_shape = jax.ShapeDtypeStruct((4,) + shards["w_in"].shape, _CD)
    send_i, recv_i, in_src, in_land, token_in = _split_copy_start(
        [shards["w_in"]], [in_shape], _gather_copies, norm_mix_g, name="gather_in_start")
    late = BIG[1:]
    send_g, recv_g, late_src, late_land, token = _split_copy_start(
        [shards[n] for n in late], [jax.ShapeDtypeStruct((4,) + shards[n].shape, _CD) for n in late],
        _gather_whole_copies, token_in, name="gather_late_start")
    adam_in = [t[0] + token_in[0, 0] for t in (w_in, m_w_in, v_w_in)]
    p = dict(norm_mix_g=norm_mix_g, b_fgt=jnp.pad(b_fgt, ((0, 0), (0, F_PAD - N_FOX_HEADS))), b_gate=b_gate,
             norm_ffn_g=norm_ffn_g, norm_final_g=norm_final_g.reshape(1, D_MODEL))

    def first_weights(after):
        own, lands = _split_copy_wait(send_i, recv_i, in_src, in_land, _gather_copies, [after] + adam_in,
                                      name="gather_in_wait")
        (g_in,) = _forward_halves(lands, name="gather_in_forward")
        blocks = lax.dynamic_update_index_in_dim(g_in, own[0], chip, 0)
        qkv, f, g = _blocks_to_pieces(blocks, (QKV_COLS, N_FOX_HEADS, 2 * D_MODEL))
        return dict(qkv=qkv, f=jnp.pad(f, ((0, 0), (0, F_PAD - N_FOX_HEADS))), g=g)

    def late_weights(after):
        own, lands = _split_copy_wait(send_g, recv_g, late_src, late_land, _gather_whole_copies, after,
                                      name="gather_late_wait")
        g_dil, g_fox, g_out, g_ffn_in, g_ffn_down = [
            lax.dynamic_update_index_in_dim(l, s, chip, 0) for l, s in zip(lands, own)]
        return dict(dil_out=_blocks_to_columns(g_dil), fox_out=_blocks_to_columns(g_fox),
                    out=g_out.reshape(D_MODEL, D_MODEL), ffn_in=g_ffn_in,
                    ffn_down=g_ffn_down.reshape(D_FF, D_MODEL))

    def to_blocks(n, full):
        shape = weights[n].shape
        if full.ndim == 3:
            return full
        if n in ("w_out", "w_ffn_down"):
            return full.reshape(4, shape[1], shape[2])
        return _columns_to_blocks(full, shape[2])

    in_flight = {}

    def grad_sink(group, gw):
        if group == "in":
            named = {"w_in": _pieces_to_blocks([gw["qkv"], gw["f"][:, :N_FOX_HEADS], gw["g"]], weights["w_in"].shape[2])}
        else:
            named = {"w_" + k: v for k, v in gw.items()}
        srcs = [to_blocks(n, named[n]) for n in named]
        lands = [jax.ShapeDtypeStruct((7, s.shape[1] // 2, s.shape[2]), s.dtype) for s in srcs]
        started = _split_copy_start(srcs, lands, _scatter_all_copies, next(iter(gw.values())),
                                    name=f"scatter_{group}_start")
        in_flight[group] = (list(named), started)
        return started[-1]

    loss_part, grad_x, gw, small = _layer_step(x[0], loss_target[0], {}, p, late_weights, grad_sink,
                                               (token_in, token), first_weights)

    where = jnp.stack([chip, c]).astype(jnp.int32)

    def summed_halves(groups, after, name):
        halves = {}
        for group in groups:
            names, (send_s, recv_s, srcs, lands, _) = in_flight[group]
            srcs, recv = _split_copy_wait(send_s, recv_s, srcs, lands, _scatter_all_copies, after,
                                          name=f"scatter_{group}_wait")
            halves.update({n: _add_all(s, r, where, name=f"add_all_{n}") for n, s, r in zip(names, srcs, recv)})
        return {n: (h, o) for (n, h), o in zip(halves.items(), _share_halves(list(halves.values()), name=name))}

    grad_halves = summed_halves(("ffn", "mix"), grad_x, "share_halves")

    out_g, out_d, out_m, out_v = {}, {}, {}, {}
    core = jnp.reshape(c, (1,)).astype(jnp.int32)

    def adamw_big(n, after=None):
        shape = weights[n].shape
        wmv = adam_in if n == "w_in" else [t[0] for t in (weights[n], m_in[n], v_in[n])]
        mine, theirs = grad_halves[n]
        outs = _adamw_halves(wmv[0], mine, theirs, wmv[1], wmv[2], core, name=f"adamw_{n}", after=after)
        out_g[n], out_d[n], out_m[n], out_v[n] = [t.reshape(shape) for t in outs]

    early = ("w_dil_out", "w_fox_out", "w_out", "w_ffn_down")
    for n in early:
        adamw_big(n)
    grad_halves.update(summed_halves(("in",), [grad_x] + [out_d[n] for n in early], "share_halves_in"))
    adamw_big("w_in")

    packed = jnp.concatenate([
        small["norm_mix_g"], small["b_gate"].reshape(2, D_MODEL), small["norm_ffn_g"], small["norm_final_g"],
        jnp.pad(small["b_fgt"], ((0, 0), (0, D_MODEL - F_PAD))), jnp.pad(loss_part, ((0, 0), (0, D_MODEL - 1))),
        jnp.zeros((1, D_MODEL), F32)], axis=0)
    summed = _sum_small(packed, after=out_d["w_in"])
    loss = summed[6, 0]
    adamw_big("w_ffn_in", after=summed)

    for n in SMALL:
        lo, hi = SMALL_ROWS[n]
        shape = weights[n].shape
        g2 = summed[lo:hi].reshape(1, -1)[:, :weights[n].size]
        d2, m2, v2 = _adamw(weights[n].reshape(g2.shape), g2, m_in[n].reshape(g2.shape), v_in[n].reshape(g2.shape),
                            name=f"adamw_{n}")
        out_g[n], out_d[n], out_m[n], out_v[n] = [t.reshape(shape) for t in (g2, d2, m2, v2)]
    return (loss, grad_x[None], *[out_g[n] for n in ORDER], *[out_d[n] for n in ORDER],
            *[out_m[n] for n in ORDER], *[out_v[n] for n in ORDER])
```

```python
import numpy as np
import jax
import jax.numpy as jnp
from jax import lax
from jax.experimental import pallas as pl
from jax.experimental.pallas import tpu as pltpu

F32 = jnp.float32
_CD = jnp.bfloat16

D_MODEL = 1024
HEAD_DIM = 64
DIL_PAIRS = ((128, 1), (512, 4), (2048, 16))
N_DIL_GROUPS = 3
DIL_HEADS = 4
DIL_W = 128
DIL_OUT = DIL_HEADS * HEAD_DIM
DIL_WIDTH = N_DIL_GROUPS * DIL_OUT
N_FOX_HEADS = 8
FOX_WIDTH = N_FOX_HEADS * HEAD_DIM
D_FF = 2816
QKV_COLS = 3 * DIL_WIDTH + 3 * FOX_WIDTH
F_PAD = 128
RMS_EPS = 1e-6
NEG_INF = -1e30
ATTN_SCALE = HEAD_DIM ** -0.5
ADAM_LR, ADAM_B1, ADAM_B2, ADAM_EPS, ADAM_WD, ADAM_STEP = 0.001, 0.9, 0.999, 1e-08, 0.01, 10

VMEM_LIMIT = 48 * 1024 * 1024
VMEM_LIMIT_RESIDENT = 56 * 1024 * 1024
LANES = 128
MESH = pl.DeviceIdType.MESH
HBM_SPEC = pl.BlockSpec(memory_space=pltpu.HBM)


def _pcall(body, after=None, **kw):
    if after is None:
        return pl.pallas_call(body, **kw)
    n_in = len(kw["in_specs"])
    kw["in_specs"] = list(kw["in_specs"]) + [pl.BlockSpec(memory_space=pl.ANY)]

    def tied(*refs):
        return body(*refs[:n_in], *refs[n_in + 1:])

    call = pl.pallas_call(tied, **kw)
    return lambda *args: call(*args, after)


def _params(*sem):
    return pltpu.CompilerParams(dimension_semantics=sem, vmem_limit_bytes=VMEM_LIMIT)


def _pick(dim, pref):
    t = (min(pref, dim) // 128) * 128
    while t >= 128:
        if dim % t == 0:
            return t
        t -= 128
    return dim


def _rms_bwd_store(r, x_ref, g_ref, dres_ref, o_ref, dg_ref):
    xv = x_ref[...]
    rs = lax.rsqrt(jnp.mean(xv * xv, axis=-1, keepdims=True) + RMS_EPS)
    xh = xv * rs
    dxh = r * g_ref[...]
    o_ref[...] = dres_ref[...] + rs * (dxh - xh * jnp.mean(dxh * xh, axis=-1, keepdims=True))
    part = jnp.sum(r * xh, axis=0, keepdims=True)
    first = pl.program_id(0) == 0

    @pl.when(first)
    def _():
        dg_ref[...] = part

    @pl.when(jnp.logical_not(first))
    def _():
        dg_ref[...] += part


def _d_h2(dgu, w_blocks, x, g, dres, *, name, tm=256, after=None):
    _, S, F = dgu.shape
    n, D, C = w_blocks.shape
    assert n == 4 and F == 2 * C
    nt = (((1,), (1,)), ((), ()))

    def body(dg_ref, du_ref, w_ref, x_ref, g_ref, dres_ref, o_ref, dgn_ref):
        r = None
        for k in range(n):
            a_ref = dg_ref if k < 2 else du_ref
            p = lax.dot_general(a_ref[:, (k % 2) * C:(k % 2 + 1) * C].astype(_CD), w_ref[k].astype(_CD), nt,
                                preferred_element_type=F32)
            r = p if r is None else r + p
        _rms_bwd_store(r, x_ref, g_ref, dres_ref, o_ref, dgn_ref)

    row = pl.BlockSpec((tm, D), lambda i: (i, 0))
    vec = pl.BlockSpec((1, D), lambda i: (0, 0))
    return _pcall(
        body, after, name=name, grid=(S // tm,),
        in_specs=[pl.BlockSpec((None, tm, F), lambda i: (0, i, 0)), pl.BlockSpec((None, tm, F), lambda i: (1, i, 0)),
                  pl.BlockSpec((n, D, C), lambda i: (0, 0, 0)), row, vec, row],
        out_specs=[row, vec], out_shape=[jax.ShapeDtypeStruct((S, D), F32), jax.ShapeDtypeStruct((1, D), F32)],
        compiler_params=_params("arbitrary"))(dgu, dgu, w_blocks, x, g, dres)


def _mm(a, b, *, name, ta=False, tb=False, out_dtype=F32, tm=1024, tn=512, tk=2048, after=None,
        out_blocks=None, b_halves=False, rms_bwd=None, more=()):
    K, M = a.shape if ta else a.shape[::-1]
    b_rows, b_cols = (b.shape[1], 2 * b.shape[2]) if b_halves else b.shape
    if tb:
        N, K2 = b_rows, b_cols
    else:
        K2, N = b_rows, b_cols
    assert K == K2, (a.shape, b.shape)
    tm = _pick(M, tm)
    tn = _pick(out_blocks or N, tn)
    tk = _pick(K, tk)
    nk = K // tk
    dn = (((0 if ta else 1,), (1 if tb else 0,)), ((), ()))
    has_norm = rms_bwd is not None
    if has_norm:
        tn = N
        assert not out_blocks and out_dtype == F32
    assert not more or (nk == 1 and tb and not ta)

    def body(*refs):
        a_ref, b_ref = refs[0], refs[1]
        rest = list(refs[2:])
        more_refs = [(rest.pop(0), rest.pop(0)) for _ in more]
        x_ref, g_ref, dres_ref = (rest.pop(0), rest.pop(0), rest.pop(0)) if has_norm else (None, None, None)
        o_ref = rest.pop(0)
        dg_ref = rest.pop(0) if has_norm else None
        p = lax.dot_general(a_ref[...].astype(_CD), b_ref[...].astype(_CD), dn, preferred_element_type=F32)
        for a2_ref, b2_ref in more_refs:
            p += lax.dot_general(a2_ref[...].astype(_CD), b2_ref[...].astype(_CD), dn, preferred_element_type=F32)

        def finish(r):
            if has_norm:
                _rms_bwd_store(r, x_ref, g_ref, dres_ref, o_ref, dg_ref)
            elif out_blocks:
                o_ref[0] = r.astype(out_dtype)
            else:
                o_ref[...] = r.astype(out_dtype)

        if nk == 1:
            finish(p)
        else:
            acc_ref = rest.pop(0)
            k = pl.program_id(2)

            @pl.when(k == 0)
            def _():
                acc_ref[...] = p

            @pl.when(k > 0)
            def _():
                acc_ref[...] += p

            @pl.when(k == nk - 1)
            def _():
                finish(acc_ref[...])

    a_spec = pl.BlockSpec((tk, tm), lambda i, j, k: (k, i)) if ta else pl.BlockSpec((tm, tk), lambda i, j, k: (i, k))
    if b_halves:
        nb_ = (N // 2) // tn
        b_spec = pl.BlockSpec((None, tk, tn), lambda i, j, k: (j // nb_, k, j % nb_))
    else:
        b_spec = pl.BlockSpec((tn, tk), lambda i, j, k: (j, k)) if tb else pl.BlockSpec((tk, tn), lambda i, j, k: (k, j))
    if out_blocks:
        oper = out_blocks // tn
        o_spec = pl.BlockSpec((1, tm, tn), lambda i, j, k: (j // oper, i, j % oper))
        out_shape = jax.ShapeDtypeStruct((N // out_blocks, M, out_blocks), out_dtype)
    else:
        o_spec = pl.BlockSpec((tm, tn), lambda i, j, k: (i, j))
        out_shape = jax.ShapeDtypeStruct((M, N), out_dtype)
    in_specs, args = [a_spec, b_spec], (a, b)
    for a2, b2 in more:
        assert a2.shape[0] == M and b2.shape == (N, a2.shape[1]), (a2.shape, b2.shape)
        in_specs += [pl.BlockSpec((tm, a2.shape[1]), lambda i, j, k: (i, 0)),
                     pl.BlockSpec((tn, a2.shape[1]), lambda i, j, k: (j, 0))]
        args += (a2, b2)
    out_specs, semantics = o_spec, ("parallel", "parallel", "arbitrary")
    if has_norm:
        vec = pl.BlockSpec((1, N), lambda i, j, k: (0, 0))
        in_specs += [o_spec, vec, o_spec]
        args += tuple(rms_bwd)
        out_specs, out_shape = [o_spec, vec], [out_shape, jax.ShapeDtypeStruct((1, N), F32)]
        semantics = ("arbitrary", "arbitrary", "arbitrary")
    return _pcall(
        body, after, name=name, grid=(M // tm, N // tn, nk), in_specs=in_specs, out_specs=out_specs,
        out_shape=out_shape,
        scratch_shapes=[pltpu.VMEM((tm, tn), F32)] if nk > 1 else [],
        compiler_params=_params(*semantics),
    )(*args)


def _rms_fwd(x, g, *, name, tm=512, after=None):
    S, D = x.shape

    def body(x_ref, g_ref, h_ref):
        xv = x_ref[...]
        r = lax.rsqrt(jnp.mean(xv * xv, axis=-1, keepdims=True) + RMS_EPS)
        h_ref[...] = ((xv * r) * g_ref[...]).astype(h_ref.dtype)

    row = pl.BlockSpec((tm, D), lambda i: (i, 0))
    return _pcall(body, after, name=name, grid=(S // tm,), in_specs=[row, pl.BlockSpec((1, D), lambda i: (0, 0))],
                  out_specs=row, out_shape=jax.ShapeDtypeStruct((S, D), _CD), compiler_params=_params("parallel"))(x, g)


def _ffn_down_loss(act, w_down, x1, g, tgt, *, name, tm=512):
    S, D = x1.shape
    F = act.shape[1]

    def body(a_ref, b_ref, x_ref, g_ref, t_ref, loss_ref, dx_ref, dg_ref):
        xv = x_ref[...] + jnp.dot(a_ref[...].astype(_CD), b_ref[...].astype(_CD), preferred_element_type=F32)
        gv = g_ref[...]
        r = lax.rsqrt(jnp.mean(xv * xv, axis=-1, keepdims=True) + RMS_EPS)
        xh = xv * r
        err = xh * gv - t_ref[...]
        lpart = 0.5 * jnp.sum(jnp.mean(err * err, axis=-1, keepdims=True), axis=0, keepdims=True)
        dy = err * (1.0 / D)
        dxh = dy * gv
        dx_ref[...] = r * (dxh - xh * jnp.mean(dxh * xh, axis=-1, keepdims=True))
        gpart = jnp.sum(dy * xh, axis=0, keepdims=True)

        @pl.when(pl.program_id(0) == 0)
        def _():
            loss_ref[...] = lpart
            dg_ref[...] = gpart

        @pl.when(pl.program_id(0) > 0)
        def _():
            loss_ref[...] += lpart
            dg_ref[...] += gpart

    row = pl.BlockSpec((tm, D), lambda i: (i, 0))
    vec = pl.BlockSpec((1, D), lambda i: (0, 0))
    one = pl.BlockSpec((1, 1), lambda i: (0, 0))
    return _pcall(body, name=name, grid=(S // tm,),
                  in_specs=[pl.BlockSpec((tm, F), lambda i: (i, 0)), pl.BlockSpec((F, D), lambda i: (0, 0)), row, vec, row],
                  out_specs=[one, row, vec],
                  out_shape=[jax.ShapeDtypeStruct((1, 1), F32), jax.ShapeDtypeStruct((S, D), F32),
                             jax.ShapeDtypeStruct((1, D), F32)],
                  compiler_params=_params("arbitrary"))(act, w_down, x1, g, tgt)


def _sigmoid(z):
    return 1.0 / (1.0 + jnp.exp(-z))


def _gated_mix_out(gl, bg, ya, yb, w_out, x, g, *, name, tm=512):
    S, D = ya.shape

    def body(za_ref, zb_ref, ba_ref, bb_ref, ya_ref, yb_ref, w_ref, x_ref, g_ref, m_ref, x1_ref, h_ref):
        ga = _sigmoid(za_ref[...].astype(F32) + ba_ref[...])
        gb = _sigmoid(zb_ref[...].astype(F32) + bb_ref[...])
        merged = (ga * ya_ref[...].astype(F32) + gb * yb_ref[...].astype(F32)).astype(m_ref.dtype)
        m_ref[...] = merged
        x1 = x_ref[...] + jnp.dot(merged, w_ref[...].astype(_CD), preferred_element_type=F32)
        x1_ref[...] = x1
        rs = lax.rsqrt(jnp.mean(x1 * x1, axis=-1, keepdims=True) + RMS_EPS)
        h_ref[...] = ((x1 * rs) * g_ref[...]).astype(h_ref.dtype)

    lo = pl.BlockSpec((tm, D), lambda i: (i, 0))
    hi = pl.BlockSpec((tm, D), lambda i: (i, 1))
    vlo = pl.BlockSpec((1, D), lambda i: (0, 0))
    vhi = pl.BlockSpec((1, D), lambda i: (0, 1))
    whole = pl.BlockSpec((D, D), lambda i: (0, 0))
    return _pcall(body, name=name, grid=(S // tm,), in_specs=[lo, hi, vlo, vhi, lo, lo, whole, lo, vlo],
                  out_specs=[lo, lo, lo],
                  out_shape=[jax.ShapeDtypeStruct((S, D), _CD), jax.ShapeDtypeStruct((S, D), F32),
                             jax.ShapeDtypeStruct((S, D), _CD)],
                  compiler_params=_params("parallel"))(gl, gl, bg, bg, ya, yb, w_out, x, g)


def _gate_bwd(dx1, w_out, gl, bg, ya, yb, *, name, tm=512):
    S, D = ya.shape
    nt = (((1,), (1,)), ((), ()))

    def body(dx_ref, w_ref, za_ref, zb_ref, ba_ref, bb_ref, ya_ref, yb_ref, dya_ref, dyb_ref, dgl_ref, dbg_ref):
        dmv = lax.dot_general(dx_ref[...].astype(_CD), w_ref[...].astype(_CD), nt, preferred_element_type=F32)
        ga = _sigmoid(za_ref[...].astype(F32) + ba_ref[...])
        gb = _sigmoid(zb_ref[...].astype(F32) + bb_ref[...])
        dya_ref[...] = (dmv * ga).astype(dya_ref.dtype)
        dyb_ref[...] = (dmv * gb).astype(dyb_ref.dtype)
        dza = dmv * ya_ref[...].astype(F32) * ga * (1.0 - ga)
        dzb = dmv * yb_ref[...].astype(F32) * gb * (1.0 - gb)
        dgl_ref[:, :D] = dza.astype(dgl_ref.dtype)
        dgl_ref[:, D:] = dzb.astype(dgl_ref.dtype)
        pa = jnp.sum(dza, axis=0, keepdims=True)
        pb = jnp.sum(dzb, axis=0, keepdims=True)

        @pl.when(pl.program_id(0) == 0)
        def _():
            dbg_ref[:, :D] = pa
            dbg_ref[:, D:] = pb

        @pl.when(pl.program_id(0) > 0)
        def _():
            dbg_ref[:, :D] += pa
            dbg_ref[:, D:] += pb

    lo = pl.BlockSpec((tm, D), lambda i: (i, 0))
    hi = pl.BlockSpec((tm, D), lambda i: (i, 1))
    vlo = pl.BlockSpec((1, D), lambda i: (0, 0))
    vhi = pl.BlockSpec((1, D), lambda i: (0, 1))
    wide = pl.BlockSpec((tm, 2 * D), lambda i: (i, 0))
    vwide = pl.BlockSpec((1, 2 * D), lambda i: (0, 0))
    whole = pl.BlockSpec((D, D), lambda i: (0, 0))
    return _pcall(body, name=name, grid=(S // tm,), in_specs=[lo, whole, lo, hi, vlo, vhi, lo, lo],
                  out_specs=[lo, lo, wide, vwide],
                  out_shape=[jax.ShapeDtypeStruct((S, D), _CD), jax.ShapeDtypeStruct((S, D), _CD),
                             jax.ShapeDtypeStruct((S, 2 * D), _CD), jax.ShapeDtypeStruct((1, 2 * D), F32)],
                  compiler_params=_params("arbitrary"))(dx1, w_out, gl, gl, bg, bg, ya, yb)


def _ffn_in_act(h2, w_blocks, *, name, tm=512):
    S, D = h2.shape
    _, _, C = w_blocks.shape

    def body(a_ref, bg_ref, bu_ref, g_ref, u_ref, o_ref):
        av = a_ref[...].astype(_CD)
        gv = jnp.dot(av, bg_ref[0].astype(_CD), preferred_element_type=F32)
        uv = jnp.dot(av, bu_ref[0].astype(_CD), preferred_element_type=F32)
        g_ref[...] = gv.astype(g_ref.dtype)
        u_ref[...] = uv.astype(u_ref.dtype)
        o_ref[...] = (gv * _sigmoid(gv) * uv).astype(o_ref.dtype)

    out = pl.BlockSpec((tm, C), lambda i, j: (i, j))
    shp = jax.ShapeDtypeStruct((S, 2 * C), _CD)
    return _pcall(body, name=name, grid=(S // tm, 2),
                  in_specs=[pl.BlockSpec((tm, D), lambda i, j: (i, 0)), pl.BlockSpec((1, D, C), lambda i, j: (j, 0, 0)),
                            pl.BlockSpec((1, D, C), lambda i, j: (2 + j, 0, 0))],
                  out_specs=[out, out, out], out_shape=[shp, shp, shp],
                  compiler_params=_params("parallel", "arbitrary"))(h2, w_blocks, w_blocks)


def _d_swiglu(dx, w_down, gate, up, *, name, tm=512, tn=1408):
    S, D = dx.shape
    F = w_down.shape[0]
    nt = (((1,), (1,)), ((), ()))

    def body(a_ref, b_ref, g_ref, u_ref, o_ref):
        dv = lax.dot_general(a_ref[...].astype(_CD), b_ref[...].astype(_CD), nt, preferred_element_type=F32)
        gv = g_ref[...].astype(F32)
        sg = _sigmoid(gv)
        o_ref[0] = (dv * u_ref[...].astype(F32) * (sg * (1.0 + gv * (1.0 - sg)))).astype(o_ref.dtype)
        o_ref[1] = (dv * (gv * sg)).astype(o_ref.dtype)

    tile = pl.BlockSpec((tm, tn), lambda i, j: (i, j))
    return _pcall(body, name=name, grid=(S // tm, F // tn),
                  in_specs=[pl.BlockSpec((tm, D), lambda i, j: (i, 0)), pl.BlockSpec((tn, D), lambda i, j: (j, 0)),
                            tile, tile],
                  out_specs=pl.BlockSpec((2, tm, tn), lambda i, j: (0, i, j)),
                  out_shape=jax.ShapeDtypeStruct((2, S, F), _CD),
                  compiler_params=_params("parallel", "arbitrary"))(dx, w_down, gate, up)


def _split3(x):
    hi = x.astype(jnp.bfloat16)
    r1 = x - hi.astype(F32)
    mid = r1.astype(jnp.bfloat16)
    lo = (r1 - mid.astype(F32)).astype(jnp.bfloat16)
    return hi, mid, lo


def _ones_dot_left(ones, x):
    return sum(jnp.dot(ones, p, preferred_element_type=F32) for p in _split3(x))


def _ones_dot_right(x, ones):
    return sum(jnp.dot(p, ones, preferred_element_type=F32) for p in _split3(x))


def _head_sum(x):
    n = x.shape[1]
    r = lax.broadcasted_iota(jnp.int32, (n, n), 0) // HEAD_DIM
    c = lax.broadcasted_iota(jnp.int32, (n, n), 1) // HEAD_DIM
    return _ones_dot_right(x, (r == c).astype(jnp.bfloat16))


def _log_sigmoid(z):
    e = jnp.exp(-jnp.abs(z))
    t = 1.0 + e
    log1p_e = jnp.where(t == 1.0, e, jnp.log(t) * (e / jnp.where(t == 1.0, 1.0, t - 1.0)))
    return jnp.minimum(z, 0.0) - log1p_e


def _fox_cumsum(zf, bf, *, name):
    S, W = zf.shape
    nb = S // 128

    def body(z_ref, b_ref, c_ref):
        tri = (lax.broadcasted_iota(jnp.int32, (128, 128), 0) >= lax.broadcasted_iota(jnp.int32, (128, 128), 1))
        tri = tri.astype(jnp.bfloat16)

        def step(i, carry):
            rows = pl.ds(pl.multiple_of(i * 128, 128), 128)
            lf = _log_sigmoid(z_ref[rows, :] + b_ref[...])
            cb = _ones_dot_left(tri, lf) + carry
            c_ref[rows, :] = cb
            return cb[127:128, :]

        lax.fori_loop(0, nb, step, jnp.zeros((1, W), F32))

    return _pcall(body, name=name, out_shape=jax.ShapeDtypeStruct((S, W), F32),
                  compiler_params=pltpu.CompilerParams(vmem_limit_bytes=VMEM_LIMIT))(zf, bf)


def _fox_cumsum_bwd(dc, zf, bf, *, name):
    S, W = zf.shape
    nb = S // 128

    def body(dc_ref, z_ref, b_ref, dz_ref, db_ref):
        tri = (lax.broadcasted_iota(jnp.int32, (128, 128), 0) <= lax.broadcasted_iota(jnp.int32, (128, 128), 1))
        tri = tri.astype(jnp.bfloat16)

        def step(k, carry):
            tail, acc = carry
            i = nb - 1 - k
            rows = pl.ds(pl.multiple_of(i * 128, 128), 128)
            dlf = _ones_dot_left(tri, dc_ref[rows, :]) + tail
            dz = dlf * _sigmoid(-(z_ref[rows, :] + b_ref[...]))
            dz_ref[rows, :] = dz
            return dlf[0:1, :], acc + jnp.sum(dz, axis=0, keepdims=True)

        _, acc = lax.fori_loop(0, nb, step, (jnp.zeros((1, W), F32), jnp.zeros((1, W), F32)))
        db_ref[...] = acc

    return _pcall(body, name=name,
                  out_shape=[jax.ShapeDtypeStruct((S, W), F32), jax.ShapeDtypeStruct((1, W), F32)],
                  compiler_params=pltpu.CompilerParams(vmem_limit_bytes=VMEM_LIMIT))(dc, zf, bf)


def _proj_dil(h, w_qkv, *, name, tm=1024):
    S, D = h.shape
    tn = DIL_WIDTH

    def body(a_ref, b_ref, *rest):
        outs, acc = rest[:N_DIL_GROUPS], rest[N_DIL_GROUPS]
        prod = jnp.dot(a_ref[...].astype(_CD), b_ref[...].astype(_CD), preferred_element_type=F32)
        for k in range(tn // LANES):
            acc[k] = prod[:, k * LANES:(k + 1) * LANES]
        for g, (_, d) in enumerate(DIL_PAIRS):
            for half in range(DIL_OUT // LANES):
                k = g * (DIL_OUT // LANES) + half
                cols = slice(half * LANES, (half + 1) * LANES)
                for r in range(d):
                    rows = pl.ds(r, tm // d, stride=d) if d > 1 else slice(None)
                    outs[g][0, r, :, cols] = acc[k, rows, :].astype(outs[g].dtype)

    out_specs = [pl.BlockSpec((1, d, tm // d, DIL_OUT), lambda i, j: (j, 0, i, 0)) for _, d in DIL_PAIRS]
    out_shape = [jax.ShapeDtypeStruct((3, d, S // d, DIL_OUT), _CD) for _, d in DIL_PAIRS]
    outs = _pcall(body, name=name, grid=(S // tm, 3),
                  in_specs=[pl.BlockSpec((tm, D), lambda i, j: (i, 0)), pl.BlockSpec((D, tn), lambda i, j: (0, j))],
                  out_specs=out_specs, out_shape=out_shape, scratch_shapes=[pltpu.VMEM((tn // LANES, tm, LANES), F32)],
                  compiler_params=_params("parallel", "arbitrary"))(h, w_qkv)
    return [o.reshape(3, S, DIL_OUT) for o in outs]


def _dil_start(block, S, dilation):
    sub = S // dilation
    u0 = block * DIL_W
    return (u0 % sub) * dilation + u0 // sub


def _dil_slopes(group):
    h = np.arange(1, N_DIL_GROUPS * DIL_HEADS + 1, dtype=np.float32)
    s = (np.float32(2.0) ** (np.float32(-8.0) * h / np.float32(N_DIL_GROUPS * DIL_HEADS))).astype(np.float32)
    return [float(v) for v in s.reshape(N_DIL_GROUPS, DIL_HEADS)[group]]


def _dil_tiles(i, n, blocks_per_seq):
    qi = lax.broadcasted_iota(jnp.int32, (DIL_W, 2 * DIL_W), 0)
    kj = lax.broadcasted_iota(jnp.int32, (DIL_W, 2 * DIL_W), 1)
    rel = qi + DIL_W - kj
    first = ((4 * n + i) % blocks_per_seq) == 0
    valid = jnp.logical_and(jnp.logical_and(rel >= 0, rel <= DIL_W), jnp.logical_or(kj >= DIL_W, jnp.logical_not(first)))
    return valid, rel.astype(F32)


def _dil_window(cur_ref, prev_ref, i, cols):
    if i > 0:
        return cur_ref[(i - 1) * DIL_W:(i + 1) * DIL_W, cols]
    return jnp.concatenate([prev_ref[:, cols], cur_ref[:DIL_W, cols]], axis=0)


CHUNK = 4 * DIL_W


def _dil_rows(block, S, dilation):
    start = _dil_start(block, S, dilation)
    return pl.ds(start, DIL_W, stride=dilation) if dilation > 1 else pl.ds(start, DIL_W)


def SPLIT(S):
    return (DIL_OUT // LANES, S, LANES)


def _dil_fwd(qkv, group, *, name):
    S = qkv.shape[1]
    dilation = DIL_PAIRS[group][1]
    bps = (S // dilation) // DIL_W
    slopes = _dil_slopes(group)
    nt = (((1,), (1,)), ((), ()))

    def body(q_ref, k_ref, v_ref, kp_ref, vp_ref, on_ref, ln_ref, o_ref, l_ref):
        n = pl.program_id(0)
        for i in range(4):
            valid, rel = _dil_tiles(i, n, bps)
            rows = slice(i * DIL_W, (i + 1) * DIL_W)
            for h in range(DIL_HEADS):
                cols = slice(h * HEAD_DIM, (h + 1) * HEAD_DIM)
                qh = q_ref[rows, cols]
                k2, v2 = _dil_window(k_ref, kp_ref, i, cols), _dil_window(v_ref, vp_ref, i, cols)
                s = lax.dot_general(qh, k2, nt, preferred_element_type=F32) * ATTN_SCALE - (slopes[h] * dilation) * rel
                s = jnp.where(valid, s, NEG_INF)
                m = jnp.max(s, axis=-1, keepdims=True)
                p = jnp.exp(s - m)
                den = jnp.sum(p, axis=-1, keepdims=True)
                acc = jnp.dot(p.astype(_CD), v2, preferred_element_type=F32)
                o_ref[rows, cols] = acc / den
                l_ref[rows, cols] = jnp.broadcast_to(m + jnp.log(den), (DIL_W, HEAD_DIM))
        for i in range(4):
            rows = slice(i * DIL_W, (i + 1) * DIL_W)
            nat = _dil_rows(4 * n + i, S, dilation)
            for half in range(DIL_OUT // LANES):
                cols = slice(half * LANES, (half + 1) * LANES)
                on_ref[half, nat, :] = o_ref[rows, cols]
                ln_ref[half, nat, :] = l_ref[rows, cols]

    def cur(which):
        return pl.BlockSpec((None, CHUNK, DIL_OUT), lambda n: (which, n, 0))

    def prev(which):
        return pl.BlockSpec((None, DIL_W, DIL_OUT), lambda n: (which, jnp.maximum(4 * n - 1, 0), 0))

    whole = pl.BlockSpec(SPLIT(S), lambda n: (0, 0, 0))
    return _pcall(body, name=name, grid=(S // CHUNK,), in_specs=[cur(0), cur(1), cur(2), prev(1), prev(2)],
                  out_specs=[whole, whole],
                  out_shape=[jax.ShapeDtypeStruct(SPLIT(S), F32), jax.ShapeDtypeStruct(SPLIT(S), F32)],
                  scratch_shapes=[pltpu.VMEM((CHUNK, DIL_OUT), F32), pltpu.VMEM((CHUNK, DIL_OUT), F32)],
                  compiler_params=_params("arbitrary"))(qkv, qkv, qkv, qkv, qkv)


STAT_OFFSET = HEAD_DIM // 2


def _dil_bwd(qkv, stats, do, group, *, name):
    S = qkv.shape[1]
    dilation = DIL_PAIRS[group][1]
    bps = (S // dilation) // DIL_W
    slopes = _dil_slopes(group)
    nchunk = S // CHUNK
    nt = (((1,), (1,)), ((), ()))
    tn = (((0,), (0,)), ((), ()))

    def body(q_ref, k_ref, v_ref, kp_ref, vp_ref, ln_ref, don_ref, dqn_ref, dkn_ref, dvn_ref,
             dk_s, dv_s, l_ref, do_ref, dq_ref):
        step = pl.program_id(0)
        n = nchunk - 1 - step
        for i in range(4):
            rows = slice(i * DIL_W, (i + 1) * DIL_W)
            nat = _dil_rows(4 * n + i, S, dilation)
            for half in range(DIL_OUT // LANES):
                cols = slice(half * LANES, (half + 1) * LANES)
                l_ref[rows, cols] = ln_ref[half, nat, :]
                do_ref[rows, cols] = don_ref[half, nat, :]

        @pl.when(step == 0)
        def _():
            dk_s[:, CHUNK:] = jnp.zeros((DIL_OUT, DIL_W), F32)
            dv_s[:, CHUNK:] = jnp.zeros((DIL_OUT, DIL_W), F32)

        dk_s[:, :CHUNK] = jnp.zeros((DIL_OUT, CHUNK), F32)
        dv_s[:, :CHUNK] = jnp.zeros((DIL_OUT, CHUNK), F32)
        for i in range(4):
            valid, rel = _dil_tiles(i, n, bps)
            rows = slice(i * DIL_W, (i + 1) * DIL_W)
            window = slice(i * DIL_W, (i + 2) * DIL_W)
            for h in range(DIL_HEADS):
                cols = slice(h * HEAD_DIM, (h + 1) * HEAD_DIM)
                qh = q_ref[rows, cols]
                k2, v2 = _dil_window(k_ref, kp_ref, i, cols), _dil_window(v_ref, vp_ref, i, cols)
                lh = l_ref[rows, h * HEAD_DIM:h * HEAD_DIM + 1]
                shift = l_ref[rows, h * HEAD_DIM + STAT_OFFSET:h * HEAD_DIM + STAT_OFFSET + 1]
                s = lax.dot_general(qh, k2, nt, preferred_element_type=F32) * ATTN_SCALE - (slopes[h] * dilation) * rel
                p = jnp.exp(jnp.where(valid, s, NEG_INF) - lh)
                dob = do_ref[rows, cols].astype(_CD)
                ds = p * (lax.dot_general(dob, v2, nt, preferred_element_type=F32) + shift)
                dsb = (ds * ATTN_SCALE).astype(_CD)
                dq_ref[rows, cols] = jnp.dot(dsb, k2, preferred_element_type=F32)
                dk_s[cols, window] += lax.dot_general(qh, dsb, tn, preferred_element_type=F32)
                dv_s[cols, window] += lax.dot_general(dob, p.astype(_CD), tn, preferred_element_type=F32)
        for i in range(4):
            rows = slice(i * DIL_W, (i + 1) * DIL_W)
            done = slice((i + 1) * DIL_W, (i + 2) * DIL_W)
            nat = _dil_rows(4 * n + i, S, dilation)
            dkb, dvb = dk_s[:, done].T, dv_s[:, done].T
            for half in range(DIL_OUT // LANES):
                cols = slice(half * LANES, (half + 1) * LANES)
                dqn_ref[half, nat, :] = dq_ref[rows, cols]
                dkn_ref[half, nat, :] = dkb[:, cols]
                dvn_ref[half, nat, :] = dvb[:, cols]
        dk_s[:, CHUNK:] = dk_s[:, :DIL_W]
        dv_s[:, CHUNK:] = dv_s[:, :DIL_W]

    def cur(which):
        return pl.BlockSpec((None, CHUNK, DIL_OUT), lambda s: (which, nchunk - 1 - s, 0))

    def prev(which):
        return pl.BlockSpec((None, DIL_W, DIL_OUT), lambda s: (which, jnp.maximum(4 * (nchunk - 1 - s) - 1, 0), 0))

    whole = pl.BlockSpec(SPLIT(S), lambda s: (0, 0, 0))
    shp = jax.ShapeDtypeStruct(SPLIT(S), F32)
    tile = pltpu.VMEM((CHUNK, DIL_OUT), F32)
    return _pcall(body, name=name, grid=(nchunk,),
                  in_specs=[cur(0), cur(1), cur(2), prev(1), prev(2), whole, whole],
                  out_specs=[whole, whole, whole], out_shape=[shp, shp, shp],
                  scratch_shapes=[pltpu.VMEM((DIL_OUT, CHUNK + DIL_W), F32), pltpu.VMEM((DIL_OUT, CHUNK + DIL_W), F32),
                                  tile, tile, tile],
                  compiler_params=pltpu.CompilerParams(dimension_semantics=("arbitrary",),
                                                       vmem_limit_bytes=VMEM_LIMIT_RESIDENT))(
        qkv, qkv, qkv, qkv, qkv, stats, do)


def _dil_mix_fwd(os_, ls_, *, name, tm=512):
    nh, S, _ = os_[0].shape

    def body(o0, o1, o2, l0, l1, l2, out_ref):
        for half in range(nh):
            ls = [l0[half], l1[half], l2[half]]
            m = jnp.maximum(jnp.maximum(ls[0], ls[1]), ls[2])
            es = [jnp.exp(l - m) for l in ls]
            den = es[0] + es[1] + es[2]
            mixed = (es[0] * o0[half] + es[1] * o1[half] + es[2] * o2[half]) / den
            out_ref[:, half * LANES:(half + 1) * LANES] = mixed.astype(out_ref.dtype)

    halves = pl.BlockSpec((nh, tm, LANES), lambda i: (0, i, 0))
    row = pl.BlockSpec((tm, nh * LANES), lambda i: (i, 0))
    return _pcall(body, name=name, grid=(S // tm,), in_specs=[halves] * 6, out_specs=row,
                  out_shape=jax.ShapeDtypeStruct((S, nh * LANES), _CD), compiler_params=_params("parallel"))(*os_, *ls_)


def _dil_mix_bwd(doa, os_, ls_, *, name, tm=512, after=None):
    nh, S, _ = os_[0].shape

    def body(d_ref, o0, o1, o2, l0, l1, l2, do0, do1, do2, st0, st1, st2):
        first = lax.broadcasted_iota(jnp.int32, (tm, LANES), 1) % HEAD_DIM < STAT_OFFSET
        for half in range(nh):
            dv = d_ref[:, half * LANES:(half + 1) * LANES]
            ls = [l0[half], l1[half], l2[half]]
            m = jnp.maximum(jnp.maximum(ls[0], ls[1]), ls[2])
            es = [jnp.exp(l - m) for l in ls]
            den = es[0] + es[1] + es[2]
            al = [e / den for e in es]
            da = [_head_sum(dv * o[half]) for o in (o0, o1, o2)]
            mean = al[0] * da[0] + al[1] * da[1] + al[2] * da[2]
            for a, l, do_ref, st_ref in zip(al, ls, (do0, do1, do2), (st0, st1, st2)):
                do_ref[half] = a * dv
                st_ref[half] = jnp.where(first, l, -a * mean)

    halves = pl.BlockSpec((nh, tm, LANES), lambda i: (0, i, 0))
    row = pl.BlockSpec((tm, nh * LANES), lambda i: (i, 0))
    shp = jax.ShapeDtypeStruct((nh, S, LANES), F32)
    return _pcall(body, after, name=name, grid=(S // tm,), in_specs=[row] + [halves] * 6, out_specs=[halves] * 6,
                  out_shape=[shp] * 6, compiler_params=_params("parallel"))(doa, *os_, *ls_)


FOX_T = 512


PACK = 2 * HEAD_DIM
HEAD_PAIRS = N_FOX_HEADS // 2
FOX_HPS = 8
Q_BLOCK0 = 0
K_BLOCK0 = FOX_WIDTH // PACK
V_BLOCK0 = 2 * FOX_WIDTH // PACK


def _pieces(x):
    hi = x.astype(jnp.bfloat16).astype(F32)
    r = x - hi
    mid = r.astype(jnp.bfloat16).astype(F32)
    lo = (r - mid).astype(jnp.bfloat16).astype(F32)
    return [hi, mid, lo]


def _extras(first, second, rows):
    lane = lax.broadcasted_iota(jnp.int32, (rows, HEAD_DIM), 1)
    out = jnp.zeros((rows, HEAD_DIM), F32)
    for base, triple in ((0, first), (3, second)):
        if all(isinstance(v, float) for v in triple) and len(set(triple)) == 1:
            if triple[0] != 0.0:
                out = jnp.where(jnp.logical_and(lane >= base, lane < base + 3), triple[0], out)
        else:
            for idx, val in enumerate(triple):
                out = jnp.where(lane == base + idx, val, out)
    return out


def _head_column(c, h):
    lane = lax.broadcasted_iota(jnp.int32, c.shape, 1)
    return jnp.sum(jnp.where(lane == h, c, 0.0), axis=1, keepdims=True)


ONES3 = [1.0, 1.0, 1.0]
ZEROS3 = [0.0, 0.0, 0.0]


def _fox_pack_fwd(qkv, c, *, name, tm=1024):
    S = qkv.shape[0]

    def body(q_ref, k_ref, v_ref, c_ref, qo_ref, ko_ref, vo_ref):
        hp = pl.program_id(1)
        cv = c_ref[...]
        v_extras = jnp.where(lax.broadcasted_iota(jnp.int32, (tm, HEAD_DIM), 1) < 3, 1.0, 0.0).astype(vo_ref.dtype)
        for hh in range(2):
            ch = _pieces(_head_column(cv, 2 * hp + hh))
            src = slice(hh * HEAD_DIM, (hh + 1) * HEAD_DIM)
            lo = slice(hh * PACK, hh * PACK + HEAD_DIM)
            hi = slice(hh * PACK + HEAD_DIM, (hh + 1) * PACK)
            qo_ref[:, lo] = (q_ref[:, src].astype(F32) * ATTN_SCALE).astype(qo_ref.dtype)
            qo_ref[:, hi] = _extras(ch, ONES3, tm).astype(qo_ref.dtype)
            ko_ref[:, lo] = k_ref[:, src]
            ko_ref[:, hi] = _extras(ONES3, [-p for p in ch], tm).astype(ko_ref.dtype)
            vo_ref[:, lo] = v_ref[:, src]
            vo_ref[:, hi] = v_extras

    def src(block0):
        return pl.BlockSpec((tm, PACK), lambda i, hp: (i, block0 + hp))

    out = pl.BlockSpec((tm, 2 * PACK), lambda i, hp: (i, hp))
    shp = jax.ShapeDtypeStruct((S, N_FOX_HEADS * PACK), _CD)
    return _pcall(body, name=name, grid=(S // tm, HEAD_PAIRS),
                  in_specs=[src(Q_BLOCK0), src(K_BLOCK0), src(V_BLOCK0), pl.BlockSpec((tm, PACK), lambda i, hp: (i, 0))],
                  out_specs=[out, out, out], out_shape=[shp, shp, shp],
                  compiler_params=_params("parallel", "parallel"))(qkv, qkv, qkv, c)


def _fox_fwd(qp, kp, vp, *, name):
    S = qp.shape[0]
    nt = S // FOX_T
    nt_dims = (((1,), (1,)), ((), ()))
    tn_dims = (((0,), (0,)), ((), ()))

    def body(i_tab, j_tab, q_ref, k_ref, v_ref, o_ref, l_ref, m_s, acc_s):
        t = pl.program_id(1)
        i, j = i_tab[t], j_tab[t]

        @pl.when(j == 0)
        def _():
            m_s[...] = jnp.full((FOX_HPS, 1, FOX_T), NEG_INF, F32)
            acc_s[...] = jnp.zeros((FOX_HPS, PACK, FOX_T), F32)

        def tile(diagonal):
            for hh in range(FOX_HPS):
                cols = slice(hh * PACK, (hh + 1) * PACK)
                st = lax.dot_general(k_ref[:, cols], q_ref[:, cols], nt_dims, preferred_element_type=F32)
                if diagonal:
                    key = lax.broadcasted_iota(jnp.int32, (FOX_T, FOX_T), 0)
                    qry = lax.broadcasted_iota(jnp.int32, (FOX_T, FOX_T), 1)
                    st = jnp.where(key <= qry, st, NEG_INF)
                m_old = m_s[hh]
                m_new = jnp.maximum(m_old, jnp.max(st, axis=0, keepdims=True))
                pt = jnp.exp(st - m_new)
                acc_s[hh] = jnp.exp(m_old - m_new) * acc_s[hh] + lax.dot_general(
                    v_ref[:, cols], pt.astype(_CD), tn_dims, preferred_element_type=F32)
                m_s[hh] = m_new

        @pl.when(j < i)
        def _():
            tile(False)

        @pl.when(j == i)
        def _():
            tile(True)
            for hh in range(FOX_HPS):
                acc = acc_s[hh]
                den = acc[HEAD_DIM:HEAD_DIM + 1, :]
                cols = slice(hh * HEAD_DIM, (hh + 1) * HEAD_DIM)
                o_ref[:, cols] = (acc[:HEAD_DIM, :] / den).T
                l_ref[:, cols] = jnp.broadcast_to(m_s[hh] + jnp.log(den), (HEAD_DIM, FOX_T)).T

    pairs = [(i, j) for i in range(nt) for j in range(i + 1)]
    i_tab = jnp.asarray([p[0] for p in pairs], jnp.int32)
    j_tab = jnp.asarray([p[1] for p in pairs], jnp.int32)
    qs = pl.BlockSpec((FOX_T, FOX_HPS * PACK), lambda hp, t, it, jt: (it[t], hp))
    ks = pl.BlockSpec((FOX_T, FOX_HPS * PACK), lambda hp, t, it, jt: (jt[t], hp))
    os_ = pl.BlockSpec((FOX_T, FOX_HPS * HEAD_DIM), lambda hp, t, it, jt: (it[t], hp))
    shp = jax.ShapeDtypeStruct((S, FOX_WIDTH), F32)
    grid_spec = pltpu.PrefetchScalarGridSpec(
        num_scalar_prefetch=2, grid=(N_FOX_HEADS // FOX_HPS, len(pairs)), in_specs=[qs, ks, ks], out_specs=[os_, os_],
        scratch_shapes=[pltpu.VMEM((FOX_HPS, 1, FOX_T), F32), pltpu.VMEM((FOX_HPS, PACK, FOX_T), F32)])
    return _pcall(body, name=name, grid_spec=grid_spec, out_shape=[shp, shp],
                  compiler_params=_params("parallel", "arbitrary"))(i_tab, j_tab, qp, kp, vp)


def _fox_pack_bwd(qkv, c, o, lse, do, *, name, tm=1024, after=None):
    S = qkv.shape[0]

    def body(q_ref, c_ref, o_ref, l_ref, do_ref, qo_ref, do_out_ref):
        hp = pl.program_id(1)
        cv = c_ref[...]
        for hh in range(2):
            src = slice(hh * HEAD_DIM, (hh + 1) * HEAD_DIM)
            lo = slice(hh * PACK, hh * PACK + HEAD_DIM)
            hi = slice(hh * PACK + HEAD_DIM, (hh + 1) * PACK)
            shift = _head_column(cv, 2 * hp + hh) - l_ref[:, hh * HEAD_DIM:hh * HEAD_DIM + 1]
            dov = do_ref[:, src]
            dsum = jnp.sum(dov * o_ref[:, src], axis=-1, keepdims=True)
            qo_ref[:, lo] = (q_ref[:, src].astype(F32) * ATTN_SCALE).astype(qo_ref.dtype)
            qo_ref[:, hi] = _extras(_pieces(shift), ONES3, tm).astype(qo_ref.dtype)
            do_out_ref[:, lo] = dov.astype(do_out_ref.dtype)
            do_out_ref[:, hi] = _extras(_pieces(-dsum), ZEROS3, tm).astype(do_out_ref.dtype)

    pair = pl.BlockSpec((tm, PACK), lambda i, hp: (i, hp))
    out = pl.BlockSpec((tm, 2 * PACK), lambda i, hp: (i, hp))
    shp = jax.ShapeDtypeStruct((S, N_FOX_HEADS * PACK), _CD)
    return _pcall(body, after, name=name, grid=(S // tm, HEAD_PAIRS),
                  in_specs=[pl.BlockSpec((tm, PACK), lambda i, hp: (i, Q_BLOCK0 + hp)),
                            pl.BlockSpec((tm, PACK), lambda i, hp: (i, 0)), pair, pair, pair],
                  out_specs=[out, out], out_shape=[shp, shp],
                  compiler_params=_params("parallel", "parallel"))(qkv, c, o, lse, do)


def _fox_bwd(qp, kp, vp, dop, *, name):
    S = qp.shape[0]
    nt = S // FOX_T
    nt_dims = (((1,), (1,)), ((), ()))
    tn_dims = (((0,), (0,)), ((), ()))

    def body(i_tab, j_tab, q_ref, k_ref, v_ref, do_ref, dq_ref, dk_ref, dv_ref, dc_ref, dr_ref,
             dq_s, dk_s, dv_s, dc_s, dr_s):
        t = pl.program_id(1)
        i, j = i_tab[t], j_tab[t]

        @pl.when(t == 0)
        def _():
            dq_s[...] = jnp.zeros((S, FOX_HPS * PACK), F32)
            dr_s[...] = jnp.zeros((FOX_HPS, 1, S), F32)

        @pl.when(i == j)
        def _():
            dk_s[...] = jnp.zeros((FOX_T, FOX_HPS * PACK), F32)
            dv_s[...] = jnp.zeros((FOX_T, FOX_HPS * PACK), F32)
            dc_s[...] = jnp.zeros((FOX_HPS, FOX_T, 1), F32)

        def tile(diagonal):
            rows = pl.ds(pl.multiple_of(i * FOX_T, FOX_T), FOX_T)
            for hh in range(FOX_HPS):
                cols = slice(hh * PACK, (hh + 1) * PACK)
                qv, kv, vv, dov = q_ref[:, cols], k_ref[:, cols], v_ref[:, cols], do_ref[:, cols]
                pt = jnp.exp(lax.dot_general(kv, qv, nt_dims, preferred_element_type=F32))
                if diagonal:
                    key = lax.broadcasted_iota(jnp.int32, (FOX_T, FOX_T), 0)
                    qry = lax.broadcasted_iota(jnp.int32, (FOX_T, FOX_T), 1)
                    pt = jnp.where(key <= qry, pt, 0.0)
                dst = pt * lax.dot_general(vv, dov, nt_dims, preferred_element_type=F32)
                dsb = dst.astype(_CD)
                dc_s[hh] += jnp.sum(dst, axis=1, keepdims=True)
                dr_s[hh, :, rows] += jnp.sum(dst, axis=0, keepdims=True)
                dv_s[:, cols] += jnp.dot(pt.astype(_CD), dov, preferred_element_type=F32)
                dk_s[:, cols] += jnp.dot(dsb, qv, preferred_element_type=F32)
                dq_s[rows, cols] += lax.dot_general(dsb, kv, tn_dims, preferred_element_type=F32)

        @pl.when(i > j)
        def _():
            tile(False)

        @pl.when(i == j)
        def _():
            tile(True)

        @pl.when(i == nt - 1)
        def _():
            for hh in range(FOX_HPS):
                src = slice(hh * PACK, hh * PACK + HEAD_DIM)
                dst_cols = slice(hh * HEAD_DIM, (hh + 1) * HEAD_DIM)
                dk_ref[:, dst_cols] = dk_s[:, src].astype(dk_ref.dtype)
                dv_ref[:, dst_cols] = dv_s[:, src].astype(dv_ref.dtype)
                dc_ref[:, dst_cols] = jnp.broadcast_to(dc_s[hh], (FOX_T, HEAD_DIM))

        @pl.when(t == len(pairs) - 1)
        def _():
            for hh in range(FOX_HPS):
                dq_ref[:, hh * HEAD_DIM:(hh + 1) * HEAD_DIM] = (
                    dq_s[:, hh * PACK:hh * PACK + HEAD_DIM] * ATTN_SCALE).astype(dq_ref.dtype)
            dr_ref[...] = dr_s[...]

    pairs = [(i, j) for j in range(nt) for i in range(j, nt)]
    i_tab = jnp.asarray([p[0] for p in pairs], jnp.int32)
    j_tab = jnp.asarray([p[1] for p in pairs], jnp.int32)
    wide, narrow = FOX_HPS * PACK, FOX_HPS * HEAD_DIM
    qs = pl.BlockSpec((FOX_T, wide), lambda hp, t, it, jt: (it[t], hp))
    ks = pl.BlockSpec((FOX_T, wide), lambda hp, t, it, jt: (jt[t], hp))
    whole = pl.BlockSpec((S, narrow), lambda hp, t, it, jt: (0, hp))
    cs = pl.BlockSpec((FOX_T, narrow), lambda hp, t, it, jt: (jt[t], hp))
    rs = pl.BlockSpec((FOX_HPS, 1, S), lambda hp, t, it, jt: (hp, 0, 0))
    shp = jax.ShapeDtypeStruct((S, FOX_WIDTH), _CD)
    grid_spec = pltpu.PrefetchScalarGridSpec(
        num_scalar_prefetch=2, grid=(N_FOX_HEADS // FOX_HPS, len(pairs)), in_specs=[qs, ks, ks, qs],
        out_specs=[whole, cs, cs, cs, rs],
        scratch_shapes=[pltpu.VMEM((S, wide), F32), pltpu.VMEM((FOX_T, wide), F32),
                        pltpu.VMEM((FOX_T, wide), F32), pltpu.VMEM((FOX_HPS, FOX_T, 1), F32),
                        pltpu.VMEM((FOX_HPS, 1, S), F32)])
    return _pcall(body, name=name, grid_spec=grid_spec,
                  out_shape=[shp, shp, shp, jax.ShapeDtypeStruct((S, FOX_WIDTH), F32),
                             jax.ShapeDtypeStruct((N_FOX_HEADS, 1, S), F32)],
                  compiler_params=_params("parallel", "arbitrary"))(i_tab, j_tab, qp, kp, vp, dop)


def _layer_step(x, tgt, w, p, late_weights=None, grad_sink=None, after=None, first_weights=None):
    S = x.shape[0]
    after_norm, after_proj = after if after is not None else (None, None)
    h = _rms_fwd(x, p["norm_mix_g"], name="rms_mix", after=after_norm)
    if first_weights is not None:
        w = {**w, **first_weights(h)}
    qkv = _mm(h, w["qkv"][:, 3 * DIL_WIDTH:], name="proj_fox", out_dtype=_CD, tn=768, tm=2048, after=after_proj)
    dil_qkv = _proj_dil(h, w["qkv"], name="proj_dil")
    zf = _mm(h, w["f"], name="proj_f")
    gl = _mm(h, w["g"], name="proj_gate", tn=1024, out_dtype=_CD)

    dil_o, dil_l = [], []
    for g in range(N_DIL_GROUPS):
        og, lg = _dil_fwd(dil_qkv[g], g, name=f"dil_fwd{g}")
        dil_o.append(og), dil_l.append(lg)
    o_a = _dil_mix_fwd(dil_o, dil_l, name="dil_mix")

    c = _fox_cumsum(zf, p["b_fgt"], name="fox_cumsum")
    fqp, fkp, fvp = _fox_pack_fwd(qkv, c, name="fox_pack")
    o_b, flse = _fox_fwd(fqp, fkp, fvp, name="fox_fwd")

    if late_weights is not None:
        w = {**w, **late_weights(o_b)}
    y_a = _mm(o_a, w["dil_out"], name="y_a", tn=1024, out_dtype=_CD)
    y_b = _mm(o_b, w["fox_out"], name="y_b", tn=1024, out_dtype=_CD)
    merged, x1, h2 = _gated_mix_out(gl, p["b_gate"], y_a, y_b, w["out"], x, p["norm_ffn_g"], name="mix_out")
    gate, up, act = _ffn_in_act(h2, w["ffn_in"], name="ffn_in")
    loss, dx2, dg_final = _ffn_down_loss(act, w["ffn_down"], x1, p["norm_final_g"], tgt, name="ffn_down_loss")

    gw_ffn_down = _mm(act, dx2, name="gw_ffn_down", ta=True, out_dtype=_CD, tm=1408)
    dgu = _d_swiglu(dx2, w["ffn_down"], gate, up, name="d_swiglu")
    gw_ffn_in = _mm(h2, dgu, name="gw_ffn_in", ta=True, out_dtype=_CD, tn=1408, out_blocks=1408, b_halves=True)
    sink = grad_sink if grad_sink is not None else (lambda group, grads: None)
    tok = sink("ffn", dict(ffn_in=gw_ffn_in, ffn_down=gw_ffn_down))
    dx1, dg_ffn = _d_h2(dgu, w["ffn_in"], x1, p["norm_ffn_g"], dx2, name="d_h2", after=tok)

    gw_out = _mm(merged, dx1, name="gw_out", ta=True, out_dtype=_CD)
    dy_a, dy_b, dgl, db_gate = _gate_bwd(dx1, w["out"], gl, p["b_gate"], y_a, y_b, name="gate_bwd")
    do_a = _mm(dy_a, w["dil_out"], name="d_o_a", tb=True)
    gw_dil_out = _mm(o_a, dy_a, name="gw_dil_out", ta=True, out_dtype=_CD, tn=1024)
    do_b = _mm(dy_b, w["fox_out"], name="d_o_b", tb=True)
    gw_fox_out = _mm(o_b, dy_b, name="gw_fox_out", ta=True, out_dtype=_CD, tn=1024)
    tok = sink("mix", dict(dil_out=gw_dil_out, fox_out=gw_fox_out, out=gw_out))

    bqp, bdop = _fox_pack_bwd(qkv, c, o_b, flse, do_b, name="fox_pack_bwd", after=tok)
    dqp, dkp, dvp, dck, dcq = _fox_bwd(bqp, fkp, fvp, bdop, name="fox_bwd")
    dc = dcq[:, 0, :].T - dck.reshape(S, N_FOX_HEADS, HEAD_DIM)[:, :, 0]
    dc = jnp.pad(dc, ((0, 0), (0, F_PAD - N_FOX_HEADS)))
    dzf, db_fgt = _fox_cumsum_bwd(dc, zf, p["b_fgt"], name="fox_cumsum_bwd")

    douts = _dil_mix_bwd(do_a, dil_o, dil_l, name="dil_mix_bwd", after=tok)
    dqs, dks, dvs = [], [], []
    for g in range(N_DIL_GROUPS):
        dq, dk, dv = _dil_bwd(dil_qkv[g], douts[3 + g], douts[g], g, name=f"dil_bwd{g}")
        for parts, t in ((dqs, dq), (dks, dk), (dvs, dv)):
            parts.extend([t[0].astype(_CD), t[1].astype(_CD)])
    dqkv = jnp.concatenate(dqs + dks + dvs + [dqp, dkp, dvp], axis=1)

    gw_qkv = _mm(h, dqkv, name="gw_qkv", ta=True, out_dtype=_CD, tn=768, tk=S)
    gw_g = _mm(h, dgl, name="gw_gate", ta=True, out_dtype=_CD, tk=S)
    gw_f = _mm(h, dzf, name="gw_f", ta=True, out_dtype=_CD)
    tok = sink("in", dict(qkv=gw_qkv, f=gw_f, g=gw_g))
    dx, dg_mix = _mm(dqkv, w["qkv"], name="d_h", tb=True, tk=QKV_COLS, tm=256, more=((dgl, w["g"]), (dzf, w["f"])),
                     rms_bwd=(x, p["norm_mix_g"], dx1), after=tok)

    gw = dict(qkv=gw_qkv, f=gw_f, g=gw_g, dil_out=gw_dil_out, fox_out=gw_fox_out, out=gw_out, ffn_in=gw_ffn_in,
              ffn_down=gw_ffn_down)
    small = dict(norm_mix_g=dg_mix, b_fgt=db_fgt, b_gate=db_gate, norm_ffn_g=dg_ffn, norm_final_g=dg_final)
    return loss, dx, gw, small


def _position():
    return lax.axis_index("x"), lax.axis_index("y"), lax.axis_index("c")


def _other_chips(x, y):
    return [(1 - x, y), (x, 1 - y), (1 - x, 1 - y)]


ROW_TILE = 16


def _row_chunks(rows, want=4):
    n = want
    while n > 1 and rows % (n * ROW_TILE):
        n //= 2
    return n


SEM_SPEC = pl.BlockSpec(memory_space=pltpu.SEMAPHORE)
ANY_SPEC = pl.BlockSpec(memory_space=pl.ANY)
DATAFLOW = pltpu.SideEffectType.DATAFLOW_SIDE_EFFECTING


def _in_hbm(a):
    return pltpu.with_memory_space_constraint(a, pltpu.HBM)


def _split_copy_start(srcs, land_shapes, copies, after, *, name):
    n, m = len(srcs), len(land_shapes)

    def body(*refs):
        src_refs, land_refs = refs[:n], refs[n:n + m]
        send_sems, recv_sems = refs[n + m + 1], refs[n + m + 2]
        token = refs[-1]
        x, y, c = _position()
        for k, (src, dst, peer) in enumerate(copies(x, y, c, src_refs, land_refs)):
            pltpu.make_async_remote_copy(src_ref=src, dst_ref=dst, send_sem=send_sems.at[k], recv_sem=recv_sems.at[k],
                                         device_id=peer, device_id_type=MESH).start()
        token[...] = jnp.zeros_like(token)

    lands = [lax.empty(s.shape, s.dtype) for s in land_shapes]
    count = len(copies(0, 0, 0, srcs, lands))
    out = _pcall(
        body, name=name,
        out_shape=(pltpu.SemaphoreType.DMA((count,)), pltpu.SemaphoreType.DMA((count,)),
                   *[pltpu.HBM(s.shape, s.dtype) for s in srcs], *[pltpu.HBM(s.shape, s.dtype) for s in land_shapes],
                   jax.ShapeDtypeStruct((8, 128), F32)),
        in_specs=[HBM_SPEC] * (n + m) + [ANY_SPEC],
        out_specs=(SEM_SPEC, SEM_SPEC, *[HBM_SPEC] * (n + m), pl.BlockSpec(memory_space=pltpu.VMEM)),
        input_output_aliases={k: 2 + k for k in range(n + m)},
        compiler_params=pltpu.CompilerParams(has_side_effects=DATAFLOW),
    )(*[_in_hbm(s) for s in srcs], *[_in_hbm(l) for l in lands], after)
    return out[0], out[1], list(out[2:2 + n]), list(out[2 + n:2 + n + m]), out[-1]


def _split_copy_wait(send_sems, recv_sems, srcs, lands, copies, after, *, name):
    n, m = len(srcs), len(lands)

    def body(*refs):
        src_refs, land_refs = refs[:n], refs[n:n + m]
        send, recv = refs[n + m], refs[n + m + 1]
        x, y, c = _position()
        for k, (src, dst, peer) in enumerate(copies(x, y, c, src_refs, land_refs)):
            cp = pltpu.make_async_remote_copy(src_ref=src, dst_ref=dst, send_sem=send.at[k], recv_sem=recv.at[k],
                                              device_id=peer, device_id_type=MESH)
            cp.wait_send()
            cp.wait_recv()

    afters = list(after) if isinstance(after, (list, tuple)) else [after]
    out = _pcall(
        body, name=name,
        out_shape=tuple(pltpu.HBM(s.shape, s.dtype) for s in list(srcs) + list(lands)),
        in_specs=[HBM_SPEC] * (n + m) + [SEM_SPEC, SEM_SPEC] + [ANY_SPEC] * len(afters),
        out_specs=tuple([HBM_SPEC] * (n + m)),
        input_output_aliases={k: k for k in range(n + m)},
        compiler_params=pltpu.CompilerParams(has_side_effects=DATAFLOW),
    )(*srcs, *lands, send_sems, recv_sems, *afters)
    return list(out[:n]), list(out[n:])


def _gather_copies(x, y, c, shard_refs, land_refs):
    out = []
    for s, l in zip(shard_refs, land_refs):
        half = s.shape[0] // 2
        nq = _row_chunks(half)
        for cx, cy in _other_chips(x, y):
            for q in range(nq):
                rows = pl.ds(c * half + q * (half // nq), half // nq)
                out.append((s.at[rows, :], l.at[2 * x + y, rows, :], (cx, cy, c)))
    return out


def _gather_whole_copies(x, y, c, shard_refs, land_refs):
    out = []
    for s, l in zip(shard_refs, land_refs):
        nq = _row_chunks(s.shape[0])
        for cx, cy in _other_chips(x, y):
            for q in range(nq):
                rows = pl.ds(q * (s.shape[0] // nq), s.shape[0] // nq)
                out.append((s.at[rows, :], l.at[2 * x + y, rows, :], (cx, cy, c)))
    return out


def _scatter_all_copies(x, y, c, block_refs, land_refs):
    out = []
    for g, l in zip(block_refs, land_refs):
        half = g.shape[1] // 2
        nq = _row_chunks(half)
        size = half // nq
        for q in range(nq):
            rows = pl.ds((1 - c) * half + q * size, size)
            out.append((g.at[2 * x + y, rows, :], l.at[0, pl.ds(q * size, size), :], (x, y, 1 - c)))
        for r, (cx, cy) in enumerate(_other_chips(x, y)):
            for j in range(2):
                h = c if j == 0 else 1 - c
                for q in range(nq):
                    rows = pl.ds(h * half + q * size, size)
                    out.append((g.at[2 * cx + cy, rows, :], l.at[1 + 2 * r + j, pl.ds(q * size, size), :], (cx, cy, h)))
    return out


def _forward_halves(lands, *, name):
    n = len(lands)

    def body(*refs):
        ins = refs[:n]
        send_sems, recv_sems = refs[2 * n:]
        x, y, c = _position()
        copies = []
        for w in range(n):
            half = ins[w].shape[1] // 2
            for r, (cx, cy) in enumerate(_other_chips(x, y)):
                blk = ins[w].at[2 * cx + cy, pl.ds(c * half, half), :]
                cp = pltpu.make_async_remote_copy(src_ref=blk, dst_ref=blk, send_sem=send_sems.at[w, r],
                                                  recv_sem=recv_sems.at[w, r], device_id=(x, y, 1 - c),
                                                  device_id_type=MESH)
                cp.start()
                copies.append(cp)
        for w in range(n):
            half = ins[w].shape[1] // 2
            for r, (cx, cy) in enumerate(_other_chips(x, y)):
                blk = ins[w].at[2 * cx + cy, pl.ds((1 - c) * half, half), :]
                pltpu.make_async_remote_copy(src_ref=blk, dst_ref=blk, send_sem=send_sems.at[w, r],
                                             recv_sem=recv_sems.at[w, r], device_id=(x, y, 1 - c),
                                             device_id_type=MESH).wait_recv()
        for cp in copies:
            cp.wait_send()

    return _pcall(
        body, name=name, in_specs=[HBM_SPEC] * n, out_specs=[HBM_SPEC] * n,
        out_shape=[jax.ShapeDtypeStruct(l.shape, l.dtype) for l in lands],
        input_output_aliases={k: k for k in range(n)},
        scratch_shapes=[pltpu.SemaphoreType.DMA((n, 3)), pltpu.SemaphoreType.DMA((n, 3))],
    )(*lands)


def _share_halves(halves, *, name):
    n = len(halves)

    def body(*refs):
        ins, outs = refs[:n], refs[n:2 * n]
        send_sems, recv_sems = refs[2 * n:]
        x, y, c = _position()
        copies = []
        for w in range(n):
            cp = pltpu.make_async_remote_copy(src_ref=ins[w], dst_ref=outs[w], send_sem=send_sems.at[w],
                                              recv_sem=recv_sems.at[w], device_id=(x, y, 1 - c), device_id_type=MESH)
            cp.start()
            copies.append(cp)
        for cp in copies:
            cp.wait()

    return _pcall(
        body, name=name, in_specs=[HBM_SPEC] * n, out_specs=[HBM_SPEC] * n,
        out_shape=[jax.ShapeDtypeStruct(h.shape, h.dtype) for h in halves],
        scratch_shapes=[pltpu.SemaphoreType.DMA((n,)), pltpu.SemaphoreType.DMA((n,))],
    )(*halves)


def _sum_small(part, after=None):
    rows, width = part.shape

    def body(x_ref, out_ref, all_ref, send_sems, recv_sems):
        x, y, c = _position()
        me, sibling = (x, y, c), (x, y, 1 - c)
        chips = _other_chips(x, y)

        def block(px, py, pc):
            return all_ref.at[pl.ds((4 * px + 2 * py + pc) * rows, rows), :]

        def copy(k, blk, to, src=None):
            return pltpu.make_async_remote_copy(
                src_ref=block(*blk) if src is None else src, dst_ref=block(*blk), send_sem=send_sems.at[k],
                recv_sem=recv_sems.at[k], device_id=to, device_id_type=MESH)

        all_ref[pl.ds((4 * x + 2 * y + c) * rows, rows), :] = x_ref[...]
        first = [copy(0, me, sibling, src=x_ref)]
        first += [copy(1 + j, me, (*chip, c), src=x_ref) for j, chip in enumerate(chips)]
        for cp in first:
            cp.start()
        passed = [copy(4 + j, (*chip, c), sibling) for j, chip in enumerate(chips)]
        for j, chip in enumerate(chips):
            copy(1 + j, (*chip, c), me).wait_recv()
            passed[j].start()
        copy(0, sibling, me).wait_recv()
        for j, chip in enumerate(chips):
            copy(4 + j, (*chip, 1 - c), me).wait_recv()
        for cp in first + passed:
            cp.wait_send()
        total = all_ref[0:rows, :]
        for d in range(1, 8):
            total = total + all_ref[d * rows:(d + 1) * rows, :]
        out_ref[...] = total

    vm = pl.BlockSpec(memory_space=pltpu.VMEM)
    return _pcall(
        body, after, name="sum_small", in_specs=[vm], out_specs=vm, out_shape=jax.ShapeDtypeStruct((rows, width), F32),
        scratch_shapes=[pltpu.VMEM((8 * rows, width), F32), pltpu.SemaphoreType.DMA((7,)), pltpu.SemaphoreType.DMA((7,))],
    )(part)


def _row_tile(R, C, itemsize=4, budget=1 << 20):
    fits = [t for t in range(ROW_TILE, R + 1, ROW_TILE) if R % t == 0 and t * C * itemsize <= budget]
    return max(fits) if fits else R


def _add_all(g, recv, where, *, name):
    _, R, C = g.shape
    half = R // 2
    t = _row_tile(half, C)
    nb = half // t

    def body(w_ref, g_ref, r_ref, o_ref):
        total = g_ref[0].astype(F32)
        for k in range(7):
            total = total + r_ref[k].astype(F32)
        o_ref[...] = total

    grid_spec = pltpu.PrefetchScalarGridSpec(
        num_scalar_prefetch=1, grid=(nb,),
        in_specs=[pl.BlockSpec((1, t, C), lambda i, wr: (wr[0], wr[1] * nb + i, 0)),
                  pl.BlockSpec((7, t, C), lambda i, wr: (0, i, 0))],
        out_specs=pl.BlockSpec((t, C), lambda i, wr: (i, 0)))
    return _pcall(body, name=name, grid_spec=grid_spec, out_shape=jax.ShapeDtypeStruct((half, C), F32),
                  compiler_params=_params("parallel"))(where, g, recv)


def _adamw(w, g, m, v, *, name):
    R, C = w.shape
    t = _row_tile(R, C)
    c1 = 1.0 - ADAM_B1 ** ADAM_STEP
    c2 = 1.0 - ADAM_B2 ** ADAM_STEP

    def body(w_ref, g_ref, m_ref, v_ref, d_ref, nm_ref, nv_ref):
        gv = g_ref[...]
        mn = ADAM_B1 * m_ref[...] + (1.0 - ADAM_B1) * gv
        vn = ADAM_B2 * v_ref[...] + (1.0 - ADAM_B2) * (gv * gv)
        d_ref[...] = -ADAM_LR * ((mn / c1) / (jnp.sqrt(vn / c2) + ADAM_EPS) + ADAM_WD * w_ref[...])
        nm_ref[...] = mn
        nv_ref[...] = vn

    blk = pl.BlockSpec((t, C), lambda i: (i, 0))
    shp = jax.ShapeDtypeStruct((R, C), F32)
    return _pcall(body, name=name, grid=(R // t,), in_specs=[blk] * 4, out_specs=[blk] * 3, out_shape=[shp] * 3,
                  compiler_params=_params("parallel"))(w, g, m, v)


def _adamw_halves(w, mine, theirs, m, v, core, *, name, after=None):
    R, C = w.shape
    half = R // 2
    t = _row_tile(half, C)
    nbh = half // t
    c1 = 1.0 - ADAM_B1 ** ADAM_STEP
    c2 = 1.0 - ADAM_B2 ** ADAM_STEP

    def body(core_ref, w_ref, a_ref, b_ref, m_ref, v_ref, *rest):
        g_ref, d_ref, nm_ref, nv_ref = rest[-4:]
        gv = jnp.where(pl.program_id(0) // nbh == core_ref[0], a_ref[...], b_ref[...])
        mn = ADAM_B1 * m_ref[...] + (1.0 - ADAM_B1) * gv
        vn = ADAM_B2 * v_ref[...] + (1.0 - ADAM_B2) * (gv * gv)
        g_ref[...] = gv
        d_ref[...] = -ADAM_LR * ((mn / c1) / (jnp.sqrt(vn / c2) + ADAM_EPS) + ADAM_WD * w_ref[...])
        nm_ref[...] = mn
        nv_ref[...] = vn

    blk = pl.BlockSpec((t, C), lambda i, cr: (i, 0))
    hblk = pl.BlockSpec((t, C), lambda i, cr: (i % nbh, 0))
    shp = jax.ShapeDtypeStruct((R, C), F32)
    tied = [] if after is None else [after]
    grid_spec = pltpu.PrefetchScalarGridSpec(num_scalar_prefetch=1, grid=(2 * nbh,),
                                             in_specs=[blk, hblk, hblk, blk, blk] + [ANY_SPEC] * len(tied),
                                             out_specs=[blk] * 4)
    return _pcall(body, name=name, grid_spec=grid_spec, out_shape=[shp] * 4,
                  compiler_params=_params("parallel"))(core, w, mine, theirs, m, v, *tied)


BIG = ("w_in", "w_dil_out", "w_fox_out", "w_out", "w_ffn_in", "w_ffn_down")
SMALL = ("norm_mix_g", "b_fgt", "b_gate", "norm_ffn_g", "norm_final_g")
ORDER = ("norm_mix_g", "w_in", "b_fgt", "b_gate", "w_dil_out", "w_fox_out", "w_out", "norm_ffn_g", "w_ffn_in",
         "w_ffn_down", "norm_final_g")
SMALL_ROWS = {"norm_mix_g": (0, 1), "b_gate": (1, 3), "norm_ffn_g": (3, 4), "norm_final_g": (4, 5), "b_fgt": (5, 6)}


def _columns_to_blocks(full, ncol):
    K = full.shape[0]
    return full.reshape(K, 4, ncol).transpose(1, 0, 2)


def _pieces_to_blocks(pieces, ncol):
    spans, start = [], 0
    for piece in pieces:
        spans.append((piece, start, start + piece.shape[1]))
        start += piece.shape[1]
    assert start == 4 * ncol
    blocks = []
    for k in range(4):
        lo, hi = k * ncol, (k + 1) * ncol
        parts = [p[:, max(lo, a) - a:min(hi, b) - a] for p, a, b in spans if a < hi and b > lo]
        blocks.append(parts[0] if len(parts) == 1 else jnp.concatenate(parts, axis=1))
    return jnp.stack(blocks)


def _blocks_to_pieces(blocks, widths):
    n, K, ncol = blocks.shape
    assert sum(widths) == n * ncol
    pieces, lo = [], 0
    for width in widths:
        hi = lo + width
        parts = [blocks[k][:, max(lo, k * ncol) - k * ncol:min(hi, (k + 1) * ncol) - k * ncol]
                 for k in range(n) if k * ncol < hi and (k + 1) * ncol > lo]
        pieces.append(parts[0] if len(parts) == 1 else jnp.concatenate(parts, axis=1))
        lo = hi
    return pieces


def _blocks_to_columns(blocks):
    n, K, ncol = blocks.shape
    return blocks.transpose(1, 0, 2).reshape(K, n * ncol)


def kernel(x, norm_mix_g, w_in, b_fgt, b_gate, w_dil_out, w_fox_out, w_out, norm_ffn_g, w_ffn_in, w_ffn_down, norm_final_g, loss_target, m_norm_mix_g, m_w_in, m_b_fgt, m_b_gate, m_w_dil_out, m_w_fox_out, m_w_out, m_norm_ffn_g, m_w_ffn_in, m_w_ffn_down, m_norm_final_g, v_norm_mix_g, v_w_in, v_b_fgt, v_b_gate, v_w_dil_out, v_w_fox_out, v_w_out, v_norm_ffn_g, v_w_ffn_in, v_w_ffn_down, v_norm_final_g):
    weights = dict(norm_mix_g=norm_mix_g, w_in=w_in, b_fgt=b_fgt, b_gate=b_gate, w_dil_out=w_dil_out,
                   w_fox_out=w_fox_out, w_out=w_out, norm_ffn_g=norm_ffn_g, w_ffn_in=w_ffn_in, w_ffn_down=w_ffn_down,
                   norm_final_g=norm_final_g)
    m_in = dict(norm_mix_g=m_norm_mix_g, w_in=m_w_in, b_fgt=m_b_fgt, b_gate=m_b_gate, w_dil_out=m_w_dil_out,
                w_fox_out=m_w_fox_out, w_out=m_w_out, norm_ffn_g=m_norm_ffn_g, w_ffn_in=m_w_ffn_in,
                w_ffn_down=m_w_ffn_down, norm_final_g=m_norm_final_g)
    v_in = dict(norm_mix_g=v_norm_mix_g, w_in=v_w_in, b_fgt=v_b_fgt, b_gate=v_b_gate, w_dil_out=v_w_dil_out,
                w_fox_out=v_w_fox_out, w_out=v_w_out, norm_ffn_g=v_norm_ffn_g, w_ffn_in=v_w_ffn_in,
                w_ffn_down=v_w_ffn_down, norm_final_g=v_norm_final_g)
    c = lax.axis_index("c")
    chip = 2 * lax.axis_index("x") + lax.axis_index("y")

    shards = {n: weights[n][0].astype(_CD) for n in BIG}
    in_shape = jax.ShapeDtypeStruct((4,) + shards["w_in"].shape, _CD)
    send_i, recv_i, in_src, in_land, token_in = _split_copy_start(
        [shards["w_in"]], [in_shape], _gather_copies, norm_mix_g, name="gather_in_start")
    late = BIG[1:]
    send_g, recv_g, late_src, late_land, token = _split_copy_start(
        [shards[n] for n in late], [jax.ShapeDtypeStruct((4,) + shards[n].shape, _CD) for n in late],
        _gather_whole_copies, token_in, name="gather_late_start")
    adam_in = [t[0] + token_in[0, 0] for t in (w_in, m_w_in, v_w_in)]
    p = dict(norm_mix_g=norm_mix_g, b_fgt=jnp.pad(b_fgt, ((0, 0), (0, F_PAD - N_FOX_HEADS))), b_gate=b_gate,
             norm_ffn_g=norm_ffn_g, norm_final_g=norm_final_g.reshape(1, D_MODEL))

    def first_weights(after):
        own, lands = _split_copy_wait(send_i, recv_i, in_src, in_land, _gather_copies, [after] + adam_in,
                                      name="gather_in_wait")
        (g_in,) = _forward_halves(lands, name="gather_in_forward")
        blocks = lax.dynamic_update_index_in_dim(g_in, own[0], chip, 0)
        qkv, f, g = _blocks_to_pieces(blocks, (QKV_COLS, N_FOX_HEADS, 2 * D_MODEL))
        return dict(qkv=qkv, f=jnp.pad(f, ((0, 0), (0, F_PAD - N_FOX_HEADS))), g=g)

    def late_weights(after):
        own, lands = _split_copy_wait(send_g, recv_g, late_src, late_land, _gather_whole_copies, after,
                                      name="gather_late_wait")
        g_dil, g_fox, g_out, g_ffn_in, g_ffn_down = [
            lax.dynamic_update_index_in_dim(l, s, chip, 0) for l, s in zip(lands, own)]
        return dict(dil_out=_blocks_to_columns(g_dil), fox_out=_blocks_to_columns(g_fox),
                    out=g_out.reshape(D_MODEL, D_MODEL), ffn_in=g_ffn_in,
                    ffn_down=g_ffn_down.reshape(D_FF, D_MODEL))

    def to_blocks(n, full):
        shape = weights[n].shape
        if full.ndim == 3:
            return full
        if n in ("w_out", "w_ffn_down"):
            return full.reshape(4, shape[1], shape[2])
        return _columns_to_blocks(full, shape[2])

    in_flight = {}

    def grad_sink(group, gw):
        if group == "in":
            named = {"w_in": _pieces_to_blocks([gw["qkv"], gw["f"][:, :N_FOX_HEADS], gw["g"]], weights["w_in"].shape[2])}
        else:
            named = {"w_" + k: v for k, v in gw.items()}
        srcs = [to_blocks(n, named[n]) for n in named]
        lands = [jax.ShapeDtypeStruct((7, s.shape[1] // 2, s.shape[2]), s.dtype) for s in srcs]
        started = _split_copy_start(srcs, lands, _scatter_all_copies, next(iter(gw.values())),
                                    name=f"scatter_{group}_start")
        in_flight[group] = (list(named), started)
        return started[-1]

    loss_part, grad_x, gw, small = _layer_step(x[0], loss_target[0], {}, p, late_weights, grad_sink,
                                               (token_in, token), first_weights)

    where = jnp.stack([chip, c]).astype(jnp.int32)

    def summed_halves(groups, after, name):
        halves = {}
        for group in groups:
            names, (send_s, recv_s, srcs, lands, _) = in_flight[group]
            srcs, recv = _split_copy_wait(send_s, recv_s, srcs, lands, _scatter_all_copies, after,
                                          name=f"scatter_{group}_wait")
            halves.update({n: _add_all(s, r, where, name=f"add_all_{n}") for n, s, r in zip(names, srcs, recv)})
        return {n: (h, o) for (n, h), o in zip(halves.items(), _share_halves(list(halves.values()), name=name))}

    grad_halves = summed_halves(("ffn", "mix"), grad_x, "share_halves")

    out_g, out_d, out_m, out_v = {}, {}, {}, {}
    core = jnp.reshape(c, (1,)).astype(jnp.int32)

    def adamw_big(n, after=None):
        shape = weights[n].shape
        wmv = adam_in if n == "w_in" else [t[0] for t in (weights[n], m_in[n], v_in[n])]
        mine, theirs = grad_halves[n]
        outs = _adamw_halves(wmv[0], mine, theirs, wmv[1], wmv[2], core, name=f"adamw_{n}", after=after)
        out_g[n], out_d[n], out_m[n], out_v[n] = [t.reshape(shape) for t in outs]

    early = ("w_dil_out", "w_fox_out", "w_out", "w_ffn_down")
    for n in early:
        adamw_big(n)
    grad_halves.update(summed_halves(("in",), [grad_x] + [out_d[n] for n in early], "share_halves_in"))
    adamw_big("w_in")

    packed = jnp.concatenate([
        small["norm_mix_g"], small["b_gate"].reshape(2, D_MODEL), small["norm_ffn_g"], small["norm_final_g"],
        jnp.pad(small["b_fgt"], ((0, 0), (0, D_MODEL - F_PAD))), jnp.pad(loss_part, ((0, 0), (0, D_MODEL - 1))),
        jnp.zeros((1, D_MODEL), F32)], axis=0)
    summed = _sum_small(packed, after=out_d["w_in"])
    loss = summed[6, 0]
    adamw_big("w_ffn_in", after=summed)

    for n in SMALL:
        lo, hi = SMALL_ROWS[n]
        shape = weights[n].shape
        g2 = summed[lo:hi].reshape(1, -1)[:, :weights[n].size]
        d2, m2, v2 = _adamw(weights[n].reshape(g2.shape), g2, m_in[n].reshape(g2.shape), v_in[n].reshape(g2.shape),
                            name=f"adamw_{n}")
        out_g[n], out_d[n], out_m[n], out_v[n] = [t.reshape(shape) for t in (g2, d2, m2, v2)]
    return (loss, grad_x[None], *[out_g[n] for n in ORDER], *[out_d[n] for n in ORDER],
            *[out_m[n] for n in ORDER], *[out_v[n] for n in ORDER])
```

```python
import numpy as np
import jax
import jax.numpy as jnp
from jax import lax
from jax.experimental import pallas as pl
from jax.experimental.pallas import tpu as pltpu

F32 = jnp.float32
_CD = jnp.bfloat16

D_MODEL = 1024
HEAD_DIM = 64
DIL_PAIRS = ((128, 1), (512, 4), (2048, 16))
N_DIL_GROUPS = 3
DIL_HEADS = 4
DIL_W = 128
DIL_OUT = DIL_HEADS * HEAD_DIM
DIL_WIDTH = N_DIL_GROUPS * DIL_OUT
N_FOX_HEADS = 8
FOX_WIDTH = N_FOX_HEADS * HEAD_DIM
D_FF = 2816
QKV_COLS = 3 * DIL_WIDTH + 3 * FOX_WIDTH
F_PAD = 128
RMS_EPS = 1e-6
NEG_INF = -1e30
ATTN_SCALE = HEAD_DIM ** -0.5
ADAM_LR, ADAM_B1, ADAM_B2, ADAM_EPS, ADAM_WD, ADAM_STEP = 0.001, 0.9, 0.999, 1e-08, 0.01, 10

VMEM_LIMIT = 48 * 1024 * 1024
VMEM_LIMIT_RESIDENT = 56 * 1024 * 1024
LANES = 128
MESH = pl.DeviceIdType.MESH
HBM_SPEC = pl.BlockSpec(memory_space=pltpu.HBM)


def _pcall(body, after=None, **kw):
    if after is None:
        return pl.pallas_call(body, **kw)
    n_in = len(kw["in_specs"])
    kw["in_specs"] = list(kw["in_specs"]) + [pl.BlockSpec(memory_space=pl.ANY)]

    def tied(*refs):
        return body(*refs[:n_in], *refs[n_in + 1:])

    call = pl.pallas_call(tied, **kw)
    return lambda *args: call(*args, after)


def _params(*sem):
    return pltpu.CompilerParams(dimension_semantics=sem, vmem_limit_bytes=VMEM_LIMIT)


def _pick(dim, pref):
    t = (min(pref, dim) // 128) * 128
    while t >= 128:
        if dim % t == 0:
            return t
        t -= 128
    return dim


def _rms_bwd_store(r, x_ref, g_ref, dres_ref, o_ref, dg_ref):
    xv = x_ref[...]
    rs = lax.rsqrt(jnp.mean(xv * xv, axis=-1, keepdims=True) + RMS_EPS)
    xh = xv * rs
    dxh = r * g_ref[...]
    o_ref[...] = dres_ref[...] + rs * (dxh - xh * jnp.mean(dxh * xh, axis=-1, keepdims=True))
    part = jnp.sum(r * xh, axis=0, keepdims=True)
    first = pl.program_id(0) == 0

    @pl.when(first)
    def _():
        dg_ref[...] = part

    @pl.when(jnp.logical_not(first))
    def _():
        dg_ref[...] += part


def _d_h2(dgu, w_blocks, x, g, dres, *, name, tm=256, after=None):
    _, S, F = dgu.shape
    n, D, C = w_blocks.shape
    assert n == 4 and F == 2 * C
    nt = (((1,), (1,)), ((), ()))

    def body(dg_ref, du_ref, w_ref, x_ref, g_ref, dres_ref, o_ref, dgn_ref):
        r = None
        for k in range(n):
            a_ref = dg_ref if k < 2 else du_ref
            p = lax.dot_general(a_ref[:, (k % 2) * C:(k % 2 + 1) * C].astype(_CD), w_ref[k].astype(_CD), nt,
                                preferred_element_type=F32)
            r = p if r is None else r + p
        _rms_bwd_store(r, x_ref, g_ref, dres_ref, o_ref, dgn_ref)

    row = pl.BlockSpec((tm, D), lambda i: (i, 0))
    vec = pl.BlockSpec((1, D), lambda i: (0, 0))
    return _pcall(
        body, after, name=name, grid=(S // tm,),
        in_specs=[pl.BlockSpec((None, tm, F), lambda i: (0, i, 0)), pl.BlockSpec((None, tm, F), lambda i: (1, i, 0)),
                  pl.BlockSpec((n, D, C), lambda i: (0, 0, 0)), row, vec, row],
        out_specs=[row, vec], out_shape=[jax.ShapeDtypeStruct((S, D), F32), jax.ShapeDtypeStruct((1, D), F32)],
        compiler_params=_params("arbitrary"))(dgu, dgu, w_blocks, x, g, dres)


def _mm(a, b, *, name, ta=False, tb=False, out_dtype=F32, tm=1024, tn=512, tk=2048, after=None,
        out_blocks=None, b_halves=False, rms_bwd=None, more=()):
    K, M = a.shape if ta else a.shape[::-1]
    b_rows, b_cols = (b.shape[1], 2 * b.shape[2]) if b_halves else b.shape
    if tb:
        N, K2 = b_rows, b_cols
    else:
        K2, N = b_rows, b_cols
    assert K == K2, (a.shape, b.shape)
    tm = _pick(M, tm)
    tn = _pick(out_blocks or N, tn)
    tk = _pick(K, tk)
    nk = K // tk
    dn = (((0 if ta else 1,), (1 if tb else 0,)), ((), ()))
    has_norm = rms_bwd is not None
    if has_norm:
        tn = N
        assert not out_blocks and out_dtype == F32
    assert not more or (nk == 1 and tb and not ta)

    def body(*refs):
        a_ref, b_ref = refs[0], refs[1]
        rest = list(refs[2:])
        more_refs = [(rest.pop(0), rest.pop(0)) for _ in more]
        x_ref, g_ref, dres_ref = (rest.pop(0), rest.pop(0), rest.pop(0)) if has_norm else (None, None, None)
        o_ref = rest.pop(0)
        dg_ref = rest.pop(0) if has_norm else None
        p = lax.dot_general(a_ref[...].astype(_CD), b_ref[...].astype(_CD), dn, preferred_element_type=F32)
        for a2_ref, b2_ref in more_refs:
            p += lax.dot_general(a2_ref[...].astype(_CD), b2_ref[...].astype(_CD), dn, preferred_element_type=F32)

        def finish(r):
            if has_norm:
                _rms_bwd_store(r, x_ref, g_ref, dres_ref, o_ref, dg_ref)
            elif out_blocks:
                o_ref[0] = r.astype(out_dtype)
            else:
                o_ref[...] = r.astype(out_dtype)

        if nk == 1:
            finish(p)
        else:
            acc_ref = rest.pop(0)
            k = pl.program_id(2)

            @pl.when(k == 0)
            def _():
                acc_ref[...] = p

            @pl.when(k > 0)
            def _():
                acc_ref[...] += p

            @pl.when(k == nk - 1)
            def _():
                finish(acc_ref[...])

    a_spec = pl.BlockSpec((tk, tm), lambda i, j, k: (k, i)) if ta else pl.BlockSpec((tm, tk), lambda i, j, k: (i, k))
    if b_halves:
        nb_ = (N // 2) // tn
        b_spec = pl.BlockSpec((None, tk, tn), lambda i, j, k: (j // nb_, k, j % nb_))
    else:
        b_spec = pl.BlockSpec((tn, tk), lambda i, j, k: (j, k)) if tb else pl.BlockSpec((tk, tn), lambda i, j, k: (k, j))
    if out_blocks:
        oper = out_blocks // tn
        o_spec = pl.BlockSpec((1, tm, tn), lambda i, j, k: (j // oper, i, j % oper))
        out_shape = jax.ShapeDtypeStruct((N // out_blocks, M, out_blocks), out_dtype)
    else:
        o_spec = pl.BlockSpec((tm, tn), lambda i, j, k: (i, j))
        out_shape = jax.ShapeDtypeStruct((M, N), out_dtype)
    in_specs, args = [a_spec, b_spec], (a, b)
    for a2, b2 in more:
        assert a2.shape[0] == M and b2.shape == (N, a2.shape[1]), (a2.shape, b2.shape)
        in_specs += [pl.BlockSpec((tm, a2.shape[1]), lambda i, j, k: (i, 0)),
                     pl.BlockSpec((tn, a2.shape[1]), lambda i, j, k: (j, 0))]
        args += (a2, b2)
    out_specs, semantics = o_spec, ("parallel", "parallel", "arbitrary")
    if has_norm:
        vec = pl.BlockSpec((1, N), lambda i, j, k: (0, 0))
        in_specs += [o_spec, vec, o_spec]
        args += tuple(rms_bwd)
        out_specs, out_shape = [o_spec, vec], [out_shape, jax.ShapeDtypeStruct((1, N), F32)]
        semantics = ("arbitrary", "arbitrary", "arbitrary")
    return _pcall(
        body, after, name=name, grid=(M // tm, N // tn, nk), in_specs=in_specs, out_specs=out_specs,
        out_shape=out_shape,
        scratch_shapes=[pltpu.VMEM((tm, tn), F32)] if nk > 1 else [],
        compiler_params=_params(*semantics),
    )(*args)


def _rms_fwd(x, g, *, name, tm=512, after=None):
    S, D = x.shape

    def body(x_ref, g_ref, h_ref):
        xv = x_ref[...]
        r = lax.rsqrt(jnp.mean(xv * xv, axis=-1, keepdims=True) + RMS_EPS)
        h_ref[...] = ((xv * r) * g_ref[...]).astype(h_ref.dtype)

    row = pl.BlockSpec((tm, D), lambda i: (i, 0))
    return _pcall(body, after, name=name, grid=(S // tm,), in_specs=[row, pl.BlockSpec((1, D), lambda i: (0, 0))],
                  out_specs=row, out_shape=jax.ShapeDtypeStruct((S, D), _CD), compiler_params=_params("parallel"))(x, g)


def _ffn_down_loss(act, w_down, x1, g, tgt, *, name, tm=512):
    S, D = x1.shape
    F = act.shape[1]

    def body(a_ref, b_ref, x_ref, g_ref, t_ref, loss_ref, dx_ref, dg_ref):
        xv = x_ref[...] + jnp.dot(a_ref[...].astype(_CD), b_ref[...].astype(_CD), preferred_element_type=F32)
        gv = g_ref[...]
        r = lax.rsqrt(jnp.mean(xv * xv, axis=-1, keepdims=True) + RMS_EPS)
        xh = xv * r
        err = xh * gv - t_ref[...]
        lpart = 0.5 * jnp.sum(jnp.mean(err * err, axis=-1, keepdims=True), axis=0, keepdims=True)
        dy = err * (1.0 / D)
        dxh = dy * gv
        dx_ref[...] = r * (dxh - xh * jnp.mean(dxh * xh, axis=-1, keepdims=True))
        gpart = jnp.sum(dy * xh, axis=0, keepdims=True)

        @pl.when(pl.program_id(0) == 0)
        def _():
            loss_ref[...] = lpart
            dg_ref[...] = gpart

        @pl.when(pl.program_id(0) > 0)
        def _():
            loss_ref[...] += lpart
            dg_ref[...] += gpart

    row = pl.BlockSpec((tm, D), lambda i: (i, 0))
    vec = pl.BlockSpec((1, D), lambda i: (0, 0))
    one = pl.BlockSpec((1, 1), lambda i: (0, 0))
    return _pcall(body, name=name, grid=(S // tm,),
                  in_specs=[pl.BlockSpec((tm, F), lambda i: (i, 0)), pl.BlockSpec((F, D), lambda i: (0, 0)), row, vec, row],
                  out_specs=[one, row, vec],
                  out_shape=[jax.ShapeDtypeStruct((1, 1), F32), jax.ShapeDtypeStruct((S, D), F32),
                             jax.ShapeDtypeStruct((1, D), F32)],
                  compiler_params=_params("arbitrary"))(act, w_down, x1, g, tgt)


def _sigmoid(z):
    return 1.0 / (1.0 + jnp.exp(-z))


def _gated_mix_out(gl, bg, ya, yb, w_out, x, g, *, name, tm=512):
    S, D = ya.shape

    def body(za_ref, zb_ref, ba_ref, bb_ref, ya_ref, yb_ref, w_ref, x_ref, g_ref, m_ref, x1_ref, h_ref):
        ga = _sigmoid(za_ref[...].astype(F32) + ba_ref[...])
        gb = _sigmoid(zb_ref[...].astype(F32) + bb_ref[...])
        merged = (ga * ya_ref[...].astype(F32) + gb * yb_ref[...].astype(F32)).astype(m_ref.dtype)
        m_ref[...] = merged
        x1 = x_ref[...] + jnp.dot(merged, w_ref[...].astype(_CD), preferred_element_type=F32)
        x1_ref[...] = x1
        rs = lax.rsqrt(jnp.mean(x1 * x1, axis=-1, keepdims=True) + RMS_EPS)
        h_ref[...] = ((x1 * rs) * g_ref[...]).astype(h_ref.dtype)

    lo = pl.BlockSpec((tm, D), lambda i: (i, 0))
    hi = pl.BlockSpec((tm, D), lambda i: (i, 1))
    vlo = pl.BlockSpec((1, D), lambda i: (0, 0))
    vhi = pl.BlockSpec((1, D), lambda i: (0, 1))
    whole = pl.BlockSpec((D, D), lambda i: (0, 0))
    return _pcall(body, name=name, grid=(S // tm,), in_specs=[lo, hi, vlo, vhi, lo, lo, whole, lo, vlo],
                  out_specs=[lo, lo, lo],
                  out_shape=[jax.ShapeDtypeStruct((S, D), _CD), jax.ShapeDtypeStruct((S, D), F32),
                             jax.ShapeDtypeStruct((S, D), _CD)],
                  compiler_params=_params("parallel"))(gl, gl, bg, bg, ya, yb, w_out, x, g)


def _gate_bwd(dx1, w_out, gl, bg, ya, yb, *, name, tm=512):
    S, D = ya.shape
    nt = (((1,), (1,)), ((), ()))

    def body(dx_ref, w_ref, za_ref, zb_ref, ba_ref, bb_ref, ya_ref, yb_ref, dya_ref, dyb_ref, dgl_ref, dbg_ref):
        dmv = lax.dot_general(dx_ref[...].astype(_CD), w_ref[...].astype(_CD), nt, preferred_element_type=F32)
        ga = _sigmoid(za_ref[...].astype(F32) + ba_ref[...])
        gb = _sigmoid(zb_ref[...].astype(F32) + bb_ref[...])
        dya_ref[...] = (dmv * ga).astype(dya_ref.dtype)
        dyb_ref[...] = (dmv * gb).astype(dyb_ref.dtype)
        dza = dmv * ya_ref[...].astype(F32) * ga * (1.0 - ga)
        dzb = dmv * yb_ref[...].astype(F32) * gb * (1.0 - gb)
        dgl_ref[:, :D] = dza.astype(dgl_ref.dtype)
        dgl_ref[:, D:] = dzb.astype(dgl_ref.dtype)
        pa = jnp.sum(dza, axis=0, keepdims=True)
        pb = jnp.sum(dzb, axis=0, keepdims=True)

        @pl.when(pl.program_id(0) == 0)
        def _():
            dbg_ref[:, :D] = pa
            dbg_ref[:, D:] = pb

        @pl.when(pl.program_id(0) > 0)
        def _():
            dbg_ref[:, :D] += pa
            dbg_ref[:, D:] += pb

    lo = pl.BlockSpec((tm, D), lambda i: (i, 0))
    hi = pl.BlockSpec((tm, D), lambda i: (i, 1))
    vlo = pl.BlockSpec((1, D), lambda i: (0, 0))
    vhi = pl.BlockSpec((1, D), lambda i: (0, 1))
    wide = pl.BlockSpec((tm, 2 * D), lambda i: (i, 0))
    vwide = pl.BlockSpec((1, 2 * D), lambda i: (0, 0))
    whole = pl.BlockSpec((D, D), lambda i: (0, 0))
    return _pcall(body, name=name, grid=(S // tm,), in_specs=[lo, whole, lo, hi, vlo, vhi, lo, lo],
                  out_specs=[lo, lo, wide, vwide],
                  out_shape=[jax.ShapeDtypeStruct((S, D), _CD), jax.ShapeDtypeStruct((S, D), _CD),
                             jax.ShapeDtypeStruct((S, 2 * D), _CD), jax.ShapeDtypeStruct((1, 2 * D), F32)],
                  compiler_params=_params("arbitrary"))(dx1, w_out, gl, gl, bg, bg, ya, yb)


def _ffn_in_act(h2, w_blocks, *, name, tm=512):
    S, D = h2.shape
    _, _, C = w_blocks.shape

    def body(a_ref, bg_ref, bu_ref, g_ref, u_ref, o_ref):
        av = a_ref[...].astype(_CD)
        gv = jnp.dot(av, bg_ref[0].astype(_CD), preferred_element_type=F32)
        uv = jnp.dot(av, bu_ref[0].astype(_CD), preferred_element_type=F32)
        g_ref[...] = gv.astype(g_ref.dtype)
        u_ref[...] = uv.astype(u_ref.dtype)
        o_ref[...] = (gv * _sigmoid(gv) * uv).astype(o_ref.dtype)

    out = pl.BlockSpec((tm, C), lambda i, j: (i, j))
    shp = jax.ShapeDtypeStruct((S, 2 * C), _CD)
    return _pcall(body, name=name, grid=(S // tm, 2),
                  in_specs=[pl.BlockSpec((tm, D), lambda i, j: (i, 0)), pl.BlockSpec((1, D, C), lambda i, j: (j, 0, 0)),
                            pl.BlockSpec((1, D, C), lambda i, j: (2 + j, 0, 0))],
                  out_specs=[out, out, out], out_shape=[shp, shp, shp],
                  compiler_params=_params("parallel", "arbitrary"))(h2, w_blocks, w_blocks)


def _d_swiglu(dx, w_down, gate, up, *, name, tm=512, tn=1408):
    S, D = dx.shape
    F = w_down.shape[0]
    nt = (((1,), (1,)), ((), ()))

    def body(a_ref, b_ref, g_ref, u_ref, o_ref):
        dv = lax.dot_general(a_ref[...].astype(_CD), b_ref[...].astype(_CD), nt, preferred_element_type=F32)
        gv = g_ref[...].astype(F32)
        sg = _sigmoid(gv)
        o_ref[0] = (dv * u_ref[...].astype(F32) * (sg * (1.0 + gv * (1.0 - sg)))).astype(o_ref.dtype)
        o_ref[1] = (dv * (gv * sg)).astype(o_ref.dtype)

    tile = pl.BlockSpec((tm, tn), lambda i, j: (i, j))
    return _pcall(body, name=name, grid=(S // tm, F // tn),
                  in_specs=[pl.BlockSpec((tm, D), lambda i, j: (i, 0)), pl.BlockSpec((tn, D), lambda i, j: (j, 0)),
                            tile, tile],
                  out_specs=pl.BlockSpec((2, tm, tn), lambda i, j: (0, i, j)),
                  out_shape=jax.ShapeDtypeStruct((2, S, F), _CD),
                  compiler_params=_params("parallel", "arbitrary"))(dx, w_down, gate, up)


def _split3(x):
    hi = x.astype(jnp.bfloat16)
    r1 = x - hi.astype(F32)
    mid = r1.astype(jnp.bfloat16)
    lo = (r1 - mid.astype(F32)).astype(jnp.bfloat16)
    return hi, mid, lo


def _ones_dot_left(ones, x):
    return sum(jnp.dot(ones, p, preferred_element_type=F32) for p in _split3(x))


def _ones_dot_right(x, ones):
    return sum(jnp.dot(p, ones, preferred_element_type=F32) for p in _split3(x))


def _head_sum(x):
    n = x.shape[1]
    r = lax.broadcasted_iota(jnp.int32, (n, n), 0) // HEAD_DIM
    c = lax.broadcasted_iota(jnp.int32, (n, n), 1) // HEAD_DIM
    return _ones_dot_right(x, (r == c).astype(jnp.bfloat16))


def _log_sigmoid(z):
    e = jnp.exp(-jnp.abs(z))
    t = 1.0 + e
    log1p_e = jnp.where(t == 1.0, e, jnp.log(t) * (e / jnp.where(t == 1.0, 1.0, t - 1.0)))
    return jnp.minimum(z, 0.0) - log1p_e


def _fox_cumsum(zf, bf, *, name):
    S, W = zf.shape
    nb = S // 128

    def body(z_ref, b_ref, c_ref):
        tri = (lax.broadcasted_iota(jnp.int32, (128, 128), 0) >= lax.broadcasted_iota(jnp.int32, (128, 128), 1))
        tri = tri.astype(jnp.bfloat16)

        def step(i, carry):
            rows = pl.ds(pl.multiple_of(i * 128, 128), 128)
            lf = _log_sigmoid(z_ref[rows, :] + b_ref[...])
            cb = _ones_dot_left(tri, lf) + carry
            c_ref[rows, :] = cb
            return cb[127:128, :]

        lax.fori_loop(0, nb, step, jnp.zeros((1, W), F32))

    return _pcall(body, name=name, out_shape=jax.ShapeDtypeStruct((S, W), F32),
                  compiler_params=pltpu.CompilerParams(vmem_limit_bytes=VMEM_LIMIT))(zf, bf)


def _fox_cumsum_bwd(dc, zf, bf, *, name):
    S, W = zf.shape
    nb = S // 128

    def body(dc_ref, z_ref, b_ref, dz_ref, db_ref):
        tri = (lax.broadcasted_iota(jnp.int32, (128, 128), 0) <= lax.broadcasted_iota(jnp.int32, (128, 128), 1))
        tri = tri.astype(jnp.bfloat16)

        def step(k, carry):
            tail, acc = carry
            i = nb - 1 - k
            rows = pl.ds(pl.multiple_of(i * 128, 128), 128)
            dlf = _ones_dot_left(tri, dc_ref[rows, :]) + tail
            dz = dlf * _sigmoid(-(z_ref[rows, :] + b_ref[...]))
            dz_ref[rows, :] = dz
            return dlf[0:1, :], acc + jnp.sum(dz, axis=0, keepdims=True)

        _, acc = lax.fori_loop(0, nb, step, (jnp.zeros((1, W), F32), jnp.zeros((1, W), F32)))
        db_ref[...] = acc

    return _pcall(body, name=name,
                  out_shape=[jax.ShapeDtypeStruct((S, W), F32), jax.ShapeDtypeStruct((1, W), F32)],
                  compiler_params=pltpu.CompilerParams(vmem_limit_bytes=VMEM_LIMIT))(dc, zf, bf)


def _proj_dil(h, w_qkv, *, name, tm=1024):
    S, D = h.shape
    tn = DIL_WIDTH

    def body(a_ref, b_ref, *rest):
        outs, acc = rest[:N_DIL_GROUPS], rest[N_DIL_GROUPS]
        prod = jnp.dot(a_ref[...].astype(_CD), b_ref[...].astype(_CD), preferred_element_type=F32)
        for k in range(tn // LANES):
            acc[k] = prod[:, k * LANES:(k + 1) * LANES]
        for g, (_, d) in enumerate(DIL_PAIRS):
            for half in range(DIL_OUT // LANES):
                k = g * (DIL_OUT // LANES) + half
                cols = slice(half * LANES, (half + 1) * LANES)
                for r in range(d):
                    rows = pl.ds(r, tm // d, stride=d) if d > 1 else slice(None)
                    outs[g][0, r, :, cols] = acc[k, rows, :].astype(outs[g].dtype)

    out_specs = [pl.BlockSpec((1, d, tm // d, DIL_OUT), lambda i, j: (j, 0, i, 0)) for _, d in DIL_PAIRS]
    out_shape = [jax.ShapeDtypeStruct((3, d, S // d, DIL_OUT), _CD) for _, d in DIL_PAIRS]
    outs = _pcall(body, name=name, grid=(S // tm, 3),
                  in_specs=[pl.BlockSpec((tm, D), lambda i, j: (i, 0)), pl.BlockSpec((D, tn), lambda i, j: (0, j))],
                  out_specs=out_specs, out_shape=out_shape, scratch_shapes=[pltpu.VMEM((tn // LANES, tm, LANES), F32)],
                  compiler_params=_params("parallel", "arbitrary"))(h, w_qkv)
    return [o.reshape(3, S, DIL_OUT) for o in outs]


def _dil_start(block, S, dilation):
    sub = S // dilation
    u0 = block * DIL_W
    return (u0 % sub) * dilation + u0 // sub


def _dil_slopes(group):
    h = np.arange(1, N_DIL_GROUPS * DIL_HEADS + 1, dtype=np.float32)
    s = (np.float32(2.0) ** (np.float32(-8.0) * h / np.float32(N_DIL_GROUPS * DIL_HEADS))).astype(np.float32)
    return [float(v) for v in s.reshape(N_DIL_GROUPS, DIL_HEADS)[group]]


def _dil_tiles(i, n, blocks_per_seq):
    qi = lax.broadcasted_iota(jnp.int32, (DIL_W, 2 * DIL_W), 0)
    kj = lax.broadcasted_iota(jnp.int32, (DIL_W, 2 * DIL_W), 1)
    rel = qi + DIL_W - kj
    first = ((4 * n + i) % blocks_per_seq) == 0
    valid = jnp.logical_and(jnp.logical_and(rel >= 0, rel <= DIL_W), jnp.logical_or(kj >= DIL_W, jnp.logical_not(first)))
    return valid, rel.astype(F32)


def _dil_window(cur_ref, prev_ref, i, cols):
    if i > 0:
        return cur_ref[(i - 1) * DIL_W:(i + 1) * DIL_W, cols]
    return jnp.concatenate([prev_ref[:, cols], cur_ref[:DIL_W, cols]], axis=0)


CHUNK = 4 * DIL_W


def _dil_rows(block, S, dilation):
    start = _dil_start(block, S, dilation)
    return pl.ds(start, DIL_W, stride=dilation) if dilation > 1 else pl.ds(start, DIL_W)


def SPLIT(S):
    return (DIL_OUT // LANES, S, LANES)


def _dil_fwd(qkv, group, *, name):
    S = qkv.shape[1]
    dilation = DIL_PAIRS[group][1]
    bps = (S // dilation) // DIL_W
    slopes = _dil_slopes(group)
    nt = (((1,), (1,)), ((), ()))

    def body(q_ref, k_ref, v_ref, kp_ref, vp_ref, on_ref, ln_ref, o_ref, l_ref):
        n = pl.program_id(0)
        for i in range(4):
            valid, rel = _dil_tiles(i, n, bps)
            rows = slice(i * DIL_W, (i + 1) * DIL_W)
            for h in range(DIL_HEADS):
                cols = slice(h * HEAD_DIM, (h + 1) * HEAD_DIM)
                qh = q_ref[rows, cols]
                k2, v2 = _dil_window(k_ref, kp_ref, i, cols), _dil_window(v_ref, vp_ref, i, cols)
                s = lax.dot_general(qh, k2, nt, preferred_element_type=F32) * ATTN_SCALE - (slopes[h] * dilation) * rel
                s = jnp.where(valid, s, NEG_INF)
                m = jnp.max(s, axis=-1, keepdims=True)
                p = jnp.exp(s - m)
                den = jnp.sum(p, axis=-1, keepdims=True)
                acc = jnp.dot(p.astype(_CD), v2, preferred_element_type=F32)
                o_ref[rows, cols] = acc / den
                l_ref[rows, cols] = jnp.broadcast_to(m + jnp.log(den), (DIL_W, HEAD_DIM))
        for i in range(4):
            rows = slice(i * DIL_W, (i + 1) * DIL_W)
            nat = _dil_rows(4 * n + i, S, dilation)
            for half in range(DIL_OUT // LANES):
                cols = slice(half * LANES, (half + 1) * LANES)
                on_ref[half, nat, :] = o_ref[rows, cols]
                ln_ref[half, nat, :] = l_ref[rows, cols]

    def cur(which):
        return pl.BlockSpec((None, CHUNK, DIL_OUT), lambda n: (which, n, 0))

    def prev(which):
        return pl.BlockSpec((None, DIL_W, DIL_OUT), lambda n: (which, jnp.maximum(4 * n - 1, 0), 0))

    whole = pl.BlockSpec(SPLIT(S), lambda n: (0, 0, 0))
    return _pcall(body, name=name, grid=(S // CHUNK,), in_specs=[cur(0), cur(1), cur(2), prev(1), prev(2)],
                  out_specs=[whole, whole],
                  out_shape=[jax.ShapeDtypeStruct(SPLIT(S), F32), jax.ShapeDtypeStruct(SPLIT(S), F32)],
                  scratch_shapes=[pltpu.VMEM((CHUNK, DIL_OUT), F32), pltpu.VMEM((CHUNK, DIL_OUT), F32)],
                  compiler_params=_params("arbitrary"))(qkv, qkv, qkv, qkv, qkv)


STAT_OFFSET = HEAD_DIM // 2


def _dil_bwd(qkv, stats, do, group, *, name):
    S = qkv.shape[1]
    dilation = DIL_PAIRS[group][1]
    bps = (S // dilation) // DIL_W
    slopes = _dil_slopes(group)
    nchunk = S // CHUNK
    nt = (((1,), (1,)), ((), ()))
    tn = (((0,), (0,)), ((), ()))

    def body(q_ref, k_ref, v_ref, kp_ref, vp_ref, ln_ref, don_ref, dqn_ref, dkn_ref, dvn_ref,
             dk_s, dv_s, l_ref, do_ref, dq_ref):
        step = pl.program_id(0)
        n = nchunk - 1 - step
        for i in range(4):
            rows = slice(i * DIL_W, (i + 1) * DIL_W)
            nat = _dil_rows(4 * n + i, S, dilation)
            for half in range(DIL_OUT // LANES):
                cols = slice(half * LANES, (half + 1) * LANES)
                l_ref[rows, cols] = ln_ref[half, nat, :]
                do_ref[rows, cols] = don_ref[half, nat, :]

        @pl.when(step == 0)
        def _():
            dk_s[:, CHUNK:] = jnp.zeros((DIL_OUT, DIL_W), F32)
            dv_s[:, CHUNK:] = jnp.zeros((DIL_OUT, DIL_W), F32)

        dk_s[:, :CHUNK] = jnp.zeros((DIL_OUT, CHUNK), F32)
        dv_s[:, :CHUNK] = jnp.zeros((DIL_OUT, CHUNK), F32)
        for i in range(4):
            valid, rel = _dil_tiles(i, n, bps)
            rows = slice(i * DIL_W, (i + 1) * DIL_W)
            window = slice(i * DIL_W, (i + 2) * DIL_W)
            for h in range(DIL_HEADS):
                cols = slice(h * HEAD_DIM, (h + 1) * HEAD_DIM)
                qh = q_ref[rows, cols]
                k2, v2 = _dil_window(k_ref, kp_ref, i, cols), _dil_window(v_ref, vp_ref, i, cols)
                lh = l_ref[rows, h * HEAD_DIM:h * HEAD_DIM + 1]
                shift = l_ref[rows, h * HEAD_DIM + STAT_OFFSET:h * HEAD_DIM + STAT_OFFSET + 1]
                s = lax.dot_general(qh, k2, nt, preferred_element_type=F32) * ATTN_SCALE - (slopes[h] * dilation) * rel
                p = jnp.exp(jnp.where(valid, s, NEG_INF) - lh)
                dob = do_ref[rows, cols].astype(_CD)
                ds = p * (lax.dot_general(dob, v2, nt, preferred_element_type=F32) + shift)
                dsb = (ds * ATTN_SCALE).astype(_CD)
                dq_ref[rows, cols] = jnp.dot(dsb, k2, preferred_element_type=F32)
                dk_s[cols, window] += lax.dot_general(qh, dsb, tn, preferred_element_type=F32)
                dv_s[cols, window] += lax.dot_general(dob, p.astype(_CD), tn, preferred_element_type=F32)
        for i in range(4):
            rows = slice(i * DIL_W, (i + 1) * DIL_W)
            done = slice((i + 1) * DIL_W, (i + 2) * DIL_W)
            nat = _dil_rows(4 * n + i, S, dilation)
            dkb, dvb = dk_s[:, done].T, dv_s[:, done].T
            for half in range(DIL_OUT // LANES):
                cols = slice(half * LANES, (half + 1) * LANES)
                dqn_ref[half, nat, :] = dq_ref[rows, cols]
                dkn_ref[half, nat, :] = dkb[:, cols]
                dvn_ref[half, nat, :] = dvb[:, cols]
        dk_s[:, CHUNK:] = dk_s[:, :DIL_W]
        dv_s[:, CHUNK:] = dv_s[:, :DIL_W]

    def cur(which):
        return pl.BlockSpec((None, CHUNK, DIL_OUT), lambda s: (which, nchunk - 1 - s, 0))

    def prev(which):
        return pl.BlockSpec((None, DIL_W, DIL_OUT), lambda s: (which, jnp.maximum(4 * (nchunk - 1 - s) - 1, 0), 0))

    whole = pl.BlockSpec(SPLIT(S), lambda s: (0, 0, 0))
    shp = jax.ShapeDtypeStruct(SPLIT(S), F32)
    tile = pltpu.VMEM((CHUNK, DIL_OUT), F32)
    return _pcall(body, name=name, grid=(nchunk,),
                  in_specs=[cur(0), cur(1), cur(2), prev(1), prev(2), whole, whole],
                  out_specs=[whole, whole, whole], out_shape=[shp, shp, shp],
                  scratch_shapes=[pltpu.VMEM((DIL_OUT, CHUNK + DIL_W), F32), pltpu.VMEM((DIL_OUT, CHUNK + DIL_W), F32),
                                  tile, tile, tile],
                  compiler_params=pltpu.CompilerParams(dimension_semantics=("arbitrary",),
                                                       vmem_limit_bytes=VMEM_LIMIT_RESIDENT))(
        qkv, qkv, qkv, qkv, qkv, stats, do)


def _dil_mix_fwd(os_, ls_, *, name, tm=512):
    nh, S, _ = os_[0].shape

    def body(o0, o1, o2, l0, l1, l2, out_ref):
        for half in range(nh):
            ls = [l0[half], l1[half], l2[half]]
            m = jnp.maximum(jnp.maximum(ls[0], ls[1]), ls[2])
            es = [jnp.exp(l - m) for l in ls]
            den = es[0] + es[1] + es[2]
            mixed = (es[0] * o0[half] + es[1] * o1[half] + es[2] * o2[half]) / den
            out_ref[:, half * LANES:(half + 1) * LANES] = mixed.astype(out_ref.dtype)

    halves = pl.BlockSpec((nh, tm, LANES), lambda i: (0, i, 0))
    row = pl.BlockSpec((tm, nh * LANES), lambda i: (i, 0))
    return _pcall(body, name=name, grid=(S // tm,), in_specs=[halves] * 6, out_specs=row,
                  out_shape=jax.ShapeDtypeStruct((S, nh * LANES), _CD), compiler_params=_params("parallel"))(*os_, *ls_)


def _dil_mix_bwd(doa, os_, ls_, *, name, tm=512, after=None):
    nh, S, _ = os_[0].shape

    def body(d_ref, o0, o1, o2, l0, l1, l2, do0, do1, do2, st0, st1, st2):
        first = lax.broadcasted_iota(jnp.int32, (tm, LANES), 1) % HEAD_DIM < STAT_OFFSET
        for half in range(nh):
            dv = d_ref[:, half * LANES:(half + 1) * LANES]
            ls = [l0[half], l1[half], l2[half]]
            m = jnp.maximum(jnp.maximum(ls[0], ls[1]), ls[2])
            es = [jnp.exp(l - m) for l in ls]
            den = es[0] + es[1] + es[2]
            al = [e / den for e in es]
            da = [_head_sum(dv * o[half]) for o in (o0, o1, o2)]
            mean = al[0] * da[0] + al[1] * da[1] + al[2] * da[2]
            for a, l, do_ref, st_ref in zip(al, ls, (do0, do1, do2), (st0, st1, st2)):
                do_ref[half] = a * dv
                st_ref[half] = jnp.where(first, l, -a * mean)

    halves = pl.BlockSpec((nh, tm, LANES), lambda i: (0, i, 0))
    row = pl.BlockSpec((tm, nh * LANES), lambda i: (i, 0))
    shp = jax.ShapeDtypeStruct((nh, S, LANES), F32)
    return _pcall(body, after, name=name, grid=(S // tm,), in_specs=[row] + [halves] * 6, out_specs=[halves] * 6,
                  out_shape=[shp] * 6, compiler_params=_params("parallel"))(doa, *os_, *ls_)


FOX_T = 512


PACK = 2 * HEAD_DIM
HEAD_PAIRS = N_FOX_HEADS // 2
FOX_HPS = 8
Q_BLOCK0 = 0
K_BLOCK0 = FOX_WIDTH // PACK
V_BLOCK0 = 2 * FOX_WIDTH // PACK


def _pieces(x):
    hi = x.astype(jnp.bfloat16).astype(F32)
    r = x - hi
    mid = r.astype(jnp.bfloat16).astype(F32)
    lo = (r - mid).astype(jnp.bfloat16).astype(F32)
    return [hi, mid, lo]


def _extras(first, second, rows):
    lane = lax.broadcasted_iota(jnp.int32, (rows, HEAD_DIM), 1)
    out = jnp.zeros((rows, HEAD_DIM), F32)
    for base, triple in ((0, first), (3, second)):
        if all(isinstance(v, float) for v in triple) and len(set(triple)) == 1:
            if triple[0] != 0.0:
                out = jnp.where(jnp.logical_and(lane >= base, lane < base + 3), triple[0], out)
        else:
            for idx, val in enumerate(triple):
                out = jnp.where(lane == base + idx, val, out)
    return out


def _head_column(c, h):
    lane = lax.broadcasted_iota(jnp.int32, c.shape, 1)
    return jnp.sum(jnp.where(lane == h, c, 0.0), axis=1, keepdims=True)


ONES3 = [1.0, 1.0, 1.0]
ZEROS3 = [0.0, 0.0, 0.0]


def _fox_pack_fwd(qkv, c, *, name, tm=1024):
    S = qkv.shape[0]

    def body(q_ref, k_ref, v_ref, c_ref, qo_ref, ko_ref, vo_ref):
        hp = pl.program_id(1)
        cv = c_ref[...]
        v_extras = jnp.where(lax.broadcasted_iota(jnp.int32, (tm, HEAD_DIM), 1) < 3, 1.0, 0.0).astype(vo_ref.dtype)
        for hh in range(2):
            ch = _pieces(_head_column(cv, 2 * hp + hh))
            src = slice(hh * HEAD_DIM, (hh + 1) * HEAD_DIM)
            lo = slice(hh * PACK, hh * PACK + HEAD_DIM)
            hi = slice(hh * PACK + HEAD_DIM, (hh + 1) * PACK)
            qo_ref[:, lo] = (q_ref[:, src].astype(F32) * ATTN_SCALE).astype(qo_ref.dtype)
            qo_ref[:, hi] = _extras(ch, ONES3, tm).astype(qo_ref.dtype)
            ko_ref[:, lo] = k_ref[:, src]
            ko_ref[:, hi] = _extras(ONES3, [-p for p in ch], tm).astype(ko_ref.dtype)
            vo_ref[:, lo] = v_ref[:, src]
            vo_ref[:, hi] = v_extras

    def src(block0):
        return pl.BlockSpec((tm, PACK), lambda i, hp: (i, block0 + hp))

    out = pl.BlockSpec((tm, 2 * PACK), lambda i, hp: (i, hp))
    shp = jax.ShapeDtypeStruct((S, N_FOX_HEADS * PACK), _CD)
    return _pcall(body, name=name, grid=(S // tm, HEAD_PAIRS),
                  in_specs=[src(Q_BLOCK0), src(K_BLOCK0), src(V_BLOCK0), pl.BlockSpec((tm, PACK), lambda i, hp: (i, 0))],
                  out_specs=[out, out, out], out_shape=[shp, shp, shp],
                  compiler_params=_params("parallel", "parallel"))(qkv, qkv, qkv, c)


def _fox_fwd(qp, kp, vp, *, name):
    S = qp.shape[0]
    nt = S // FOX_T
    nt_dims = (((1,), (1,)), ((), ()))
    tn_dims = (((0,), (0,)), ((), ()))

    def body(i_tab, j_tab, q_ref, k_ref, v_ref, o_ref, l_ref, m_s, acc_s):
        t = pl.program_id(1)
        i, j = i_tab[t], j_tab[t]

        @pl.when(j == 0)
        def _():
            m_s[...] = jnp.full((FOX_HPS, 1, FOX_T), NEG_INF, F32)
            acc_s[...] = jnp.zeros((FOX_HPS, PACK, FOX_T), F32)

        def tile(diagonal):
            for hh in range(FOX_HPS):
                cols = slice(hh * PACK, (hh + 1) * PACK)
                st = lax.dot_general(k_ref[:, cols], q_ref[:, cols], nt_dims, preferred_element_type=F32)
                if diagonal:
                    key = lax.broadcasted_iota(jnp.int32, (FOX_T, FOX_T), 0)
                    qry = lax.broadcasted_iota(jnp.int32, (FOX_T, FOX_T), 1)
                    st = jnp.where(key <= qry, st, NEG_INF)
                m_old = m_s[hh]
                m_new = jnp.maximum(m_old, jnp.max(st, axis=0, keepdims=True))
                pt = jnp.exp(st - m_new)
                acc_s[hh] = jnp.exp(m_old - m_new) * acc_s[hh] + lax.dot_general(
                    v_ref[:, cols], pt.astype(_CD), tn_dims, preferred_element_type=F32)
                m_s[hh] = m_new

        @pl.when(j < i)
        def _():
            tile(False)

        @pl.when(j == i)
        def _():
            tile(True)
            for hh in range(FOX_HPS):
                acc = acc_s[hh]
                den = acc[HEAD_DIM:HEAD_DIM + 1, :]
                cols = slice(hh * HEAD_DIM, (hh + 1) * HEAD_DIM)
                o_ref[:, cols] = (acc[:HEAD_DIM, :] / den).T
                l_ref[:, cols] = jnp.broadcast_to(m_s[hh] + jnp.log(den), (HEAD_DIM, FOX_T)).T

    pairs = [(i, j) for i in range(nt) for j in range(i + 1)]
    i_tab = jnp.asarray([p[0] for p in pairs], jnp.int32)
    j_tab = jnp.asarray([p[1] for p in pairs], jnp.int32)
    qs = pl.BlockSpec((FOX_T, FOX_HPS * PACK), lambda hp, t, it, jt: (it[t], hp))
    ks = pl.BlockSpec((FOX_T, FOX_HPS * PACK), lambda hp, t, it, jt: (jt[t], hp))
    os_ = pl.BlockSpec((FOX_T, FOX_HPS * HEAD_DIM), lambda hp, t, it, jt: (it[t], hp))
    shp = jax.ShapeDtypeStruct((S, FOX_WIDTH), F32)
    grid_spec = pltpu.PrefetchScalarGridSpec(
        num_scalar_prefetch=2, grid=(N_FOX_HEADS // FOX_HPS, len(pairs)), in_specs=[qs, ks, ks], out_specs=[os_, os_],
        scratch_shapes=[pltpu.VMEM((FOX_HPS, 1, FOX_T), F32), pltpu.VMEM((FOX_HPS, PACK, FOX_T), F32)])
    return _pcall(body, name=name, grid_spec=grid_spec, out_shape=[shp, shp],
                  compiler_params=_params("parallel", "arbitrary"))(i_tab, j_tab, qp, kp, vp)


def _fox_pack_bwd(qkv, c, o, lse, do, *, name, tm=1024, after=None):
    S = qkv.shape[0]

    def body(q_ref, c_ref, o_ref, l_ref, do_ref, qo_ref, do_out_ref):
        hp = pl.program_id(1)
        cv = c_ref[...]
        for hh in range(2):
            src = slice(hh * HEAD_DIM, (hh + 1) * HEAD_DIM)
            lo = slice(hh * PACK, hh * PACK + HEAD_DIM)
            hi = slice(hh * PACK + HEAD_DIM, (hh + 1) * PACK)
            shift = _head_column(cv, 2 * hp + hh) - l_ref[:, hh * HEAD_DIM:hh * HEAD_DIM + 1]
            dov = do_ref[:, src]
            dsum = jnp.sum(dov * o_ref[:, src], axis=-1, keepdims=True)
            qo_ref[:, lo] = (q_ref[:, src].astype(F32) * ATTN_SCALE).astype(qo_ref.dtype)
            qo_ref[:, hi] = _extras(_pieces(shift), ONES3, tm).astype(qo_ref.dtype)
            do_out_ref[:, lo] = dov.astype(do_out_ref.dtype)
            do_out_ref[:, hi] = _extras(_pieces(-dsum), ZEROS3, tm).astype(do_out_ref.dtype)

    pair = pl.BlockSpec((tm, PACK), lambda i, hp: (i, hp))
    out = pl.BlockSpec((tm, 2 * PACK), lambda i, hp: (i, hp))
    shp = jax.ShapeDtypeStruct((S, N_FOX_HEADS * PACK), _CD)
    return _pcall(body, after, name=name, grid=(S // tm, HEAD_PAIRS),
                  in_specs=[pl.BlockSpec((tm, PACK), lambda i, hp: (i, Q_BLOCK0 + hp)),
                            pl.BlockSpec((tm, PACK), lambda i, hp: (i, 0)), pair, pair, pair],
                  out_specs=[out, out], out_shape=[shp, shp],
                  compiler_params=_params("parallel", "parallel"))(qkv, c, o, lse, do)


def _fox_bwd(qp, kp, vp, dop, *, name):
    S = qp.shape[0]
    nt = S // FOX_T
    nt_dims = (((1,), (1,)), ((), ()))
    tn_dims = (((0,), (0,)), ((), ()))

    def body(i_tab, j_tab, q_ref, k_ref, v_ref, do_ref, dq_ref, dk_ref, dv_ref, dc_ref, dr_ref,
             dq_s, dk_s, dv_s, dc_s, dr_s):
        t = pl.program_id(1)
        i, j = i_tab[t], j_tab[t]

        @pl.when(t == 0)
        def _():
            dq_s[...] = jnp.zeros((S, FOX_HPS * PACK), F32)
            dr_s[...] = jnp.zeros((FOX_HPS, 1, S), F32)

        @pl.when(i == j)
        def _():
            dk_s[...] = jnp.zeros((FOX_T, FOX_HPS * PACK), F32)
            dv_s[...] = jnp.zeros((FOX_T, FOX_HPS * PACK), F32)
            dc_s[...] = jnp.zeros((FOX_HPS, FOX_T, 1), F32)

        def tile(diagonal):
            rows = pl.ds(pl.multiple_of(i * FOX_T, FOX_T), FOX_T)
            for hh in range(FOX_HPS):
                cols = slice(hh * PACK, (hh + 1) * PACK)
                qv, kv, vv, dov = q_ref[:, cols], k_ref[:, cols], v_ref[:, cols], do_ref[:, cols]
                pt = jnp.exp(lax.dot_general(kv, qv, nt_dims, preferred_element_type=F32))
                if diagonal:
                    key = lax.broadcasted_iota(jnp.int32, (FOX_T, FOX_T), 0)
                    qry = lax.broadcasted_iota(jnp.int32, (FOX_T, FOX_T), 1)
                    pt = jnp.where(key <= qry, pt, 0.0)
                dst = pt * lax.dot_general(vv, dov, nt_dims, preferred_element_type=F32)
                dsb = dst.astype(_CD)
                dc_s[hh] += jnp.sum(dst, axis=1, keepdims=True)
                dr_s[hh, :, rows] += jnp.sum(dst, axis=0, keepdims=True)
                dv_s[:, cols] += jnp.dot(pt.astype(_CD), dov, preferred_element_type=F32)
                dk_s[:, cols] += jnp.dot(dsb, qv, preferred_element_type=F32)
                dq_s[rows, cols] += lax.dot_general(dsb, kv, tn_dims, preferred_element_type=F32)

        @pl.when(i > j)
        def _():
            tile(False)

        @pl.when(i == j)
        def _():
            tile(True)

        @pl.when(i == nt - 1)
        def _():
            for hh in range(FOX_HPS):
                src = slice(hh * PACK, hh * PACK + HEAD_DIM)
                dst_cols = slice(hh * HEAD_DIM, (hh + 1) * HEAD_DIM)
                dk_ref[:, dst_cols] = dk_s[:, src].astype(dk_ref.dtype)
                dv_ref[:, dst_cols] = dv_s[:, src].astype(dv_ref.dtype)
                dc_ref[:, dst_cols] = jnp.broadcast_to(dc_s[hh], (FOX_T, HEAD_DIM))

        @pl.when(t == len(pairs) - 1)
        def _():
            for hh in range(FOX_HPS):
                dq_ref[:, hh * HEAD_DIM:(hh + 1) * HEAD_DIM] = (
                    dq_s[:, hh * PACK:hh * PACK + HEAD_DIM] * ATTN_SCALE).astype(dq_ref.dtype)
            dr_ref[...] = dr_s[...]

    pairs = [(i, j) for j in range(nt) for i in range(j, nt)]
    i_tab = jnp.asarray([p[0] for p in pairs], jnp.int32)
    j_tab = jnp.asarray([p[1] for p in pairs], jnp.int32)
    wide, narrow = FOX_HPS * PACK, FOX_HPS * HEAD_DIM
    qs = pl.BlockSpec((FOX_T, wide), lambda hp, t, it, jt: (it[t], hp))
    ks = pl.BlockSpec((FOX_T, wide), lambda hp, t, it, jt: (jt[t], hp))
    whole = pl.BlockSpec((S, narrow), lambda hp, t, it, jt: (0, hp))
    cs = pl.BlockSpec((FOX_T, narrow), lambda hp, t, it, jt: (jt[t], hp))
    rs = pl.BlockSpec((FOX_HPS, 1, S), lambda hp, t, it, jt: (hp, 0, 0))
    shp = jax.ShapeDtypeStruct((S, FOX_WIDTH), _CD)
    grid_spec = pltpu.PrefetchScalarGridSpec(
        num_scalar_prefetch=2, grid=(N_FOX_HEADS // FOX_HPS, len(pairs)), in_specs=[qs, ks, ks, qs],
        out_specs=[whole, cs, cs, cs, rs],
        scratch_shapes=[pltpu.VMEM((S, wide), F32), pltpu.VMEM((FOX_T, wide), F32),
                        pltpu.VMEM((FOX_T, wide), F32), pltpu.VMEM((FOX_HPS, FOX_T, 1), F32),
                        pltpu.VMEM((FOX_HPS, 1, S), F32)])
    return _pcall(body, name=name, grid_spec=grid_spec,
                  out_shape=[shp, shp, shp, jax.ShapeDtypeStruct((S, FOX_WIDTH), F32),
                             jax.ShapeDtypeStruct((N_FOX_HEADS, 1, S), F32)],
                  compiler_params=_params("parallel", "arbitrary"))(i_tab, j_tab, qp, kp, vp, dop)


def _layer_step(x, tgt, w, p, late_weights=None, grad_sink=None, after=None, first_weights=None):
    S = x.shape[0]
    after_norm, after_proj = after if after is not None else (None, None)
    h = _rms_fwd(x, p["norm_mix_g"], name="rms_mix", after=after_norm)
    if first_weights is not None:
        w = {**w, **first_weights(h)}
    qkv = _mm(h, w["qkv"][:, 3 * DIL_WIDTH:], name="proj_fox", out_dtype=_CD, tn=768, tm=2048, after=after_proj)
    dil_qkv = _proj_dil(h, w["qkv"], name="proj_dil")
    zf = _mm(h, w["f"], name="proj_f")
    gl = _mm(h, w["g"], name="proj_gate", tn=1024, out_dtype=_CD)

    dil_o, dil_l = [], []
    for g in range(N_DIL_GROUPS):
        og, lg = _dil_fwd(dil_qkv[g], g, name=f"dil_fwd{g}")
        dil_o.append(og), dil_l.append(lg)
    o_a = _dil_mix_fwd(dil_o, dil_l, name="dil_mix")

    c = _fox_cumsum(zf, p["b_fgt"], name="fox_cumsum")
    fqp, fkp, fvp = _fox_pack_fwd(qkv, c, name="fox_pack")
    o_b, flse = _fox_fwd(fqp, fkp, fvp, name="fox_fwd")

    if late_weights is not None:
        w = {**w, **late_weights(o_b)}
    y_a = _mm(o_a, w["dil_out"], name="y_a", tn=1024, out_dtype=_CD)
    y_b = _mm(o_b, w["fox_out"], name="y_b", tn=1024, out_dtype=_CD)
    merged, x1, h2 = _gated_mix_out(gl, p["b_gate"], y_a, y_b, w["out"], x, p["norm_ffn_g"], name="mix_out")
    gate, up, act = _ffn_in_act(h2, w["ffn_in"], name="ffn_in")
    loss, dx2, dg_final = _ffn_down_loss(act, w["ffn_down"], x1, p["norm_final_g"], tgt, name="ffn_down_loss")

    gw_ffn_down = _mm(act, dx2, name="gw_ffn_down", ta=True, out_dtype=_CD, tm=1408)
    dgu = _d_swiglu(dx2, w["ffn_down"], gate, up, name="d_swiglu")
    gw_ffn_in = _mm(h2, dgu, name="gw_ffn_in", ta=True, out_dtype=_CD, tn=1408, out_blocks=1408, b_halves=True)
    sink = grad_sink if grad_sink is not None else (lambda group, grads: None)
    tok = sink("ffn", dict(ffn_in=gw_ffn_in, ffn_down=gw_ffn_down))
    dx1, dg_ffn = _d_h2(dgu, w["ffn_in"], x1, p["norm_ffn_g"], dx2, name="d_h2", after=tok)

    gw_out = _mm(merged, dx1, name="gw_out", ta=True, out_dtype=_CD)
    dy_a, dy_b, dgl, db_gate = _gate_bwd(dx1, w["out"], gl, p["b_gate"], y_a, y_b, name="gate_bwd")
    do_a = _mm(dy_a, w["dil_out"], name="d_o_a", tb=True)
    gw_dil_out = _mm(o_a, dy_a, name="gw_dil_out", ta=True, out_dtype=_CD, tn=1024)
    do_b = _mm(dy_b, w["fox_out"], name="d_o_b", tb=True)
    gw_fox_out = _mm(o_b, dy_b, name="gw_fox_out", ta=True, out_dtype=_CD, tn=1024)
    tok = sink("mix", dict(dil_out=gw_dil_out, fox_out=gw_fox_out, out=gw_out))

    bqp, bdop = _fox_pack_bwd(qkv, c, o_b, flse, do_b, name="fox_pack_bwd", after=tok)
    dqp, dkp, dvp, dck, dcq = _fox_bwd(bqp, fkp, fvp, bdop, name="fox_bwd")
    dc = dcq[:, 0, :].T - dck.reshape(S, N_FOX_HEADS, HEAD_DIM)[:, :, 0]
    dc = jnp.pad(dc, ((0, 0), (0, F_PAD - N_FOX_HEADS)))
    dzf, db_fgt = _fox_cumsum_bwd(dc, zf, p["b_fgt"], name="fox_cumsum_bwd")

    douts = _dil_mix_bwd(do_a, dil_o, dil_l, name="dil_mix_bwd", after=tok)
    dqs, dks, dvs = [], [], []
    for g in range(N_DIL_GROUPS):
        dq, dk, dv = _dil_bwd(dil_qkv[g], douts[3 + g], douts[g], g, name=f"dil_bwd{g}")
        for parts, t in ((dqs, dq), (dks, dk), (dvs, dv)):
            parts.extend([t[0].astype(_CD), t[1].astype(_CD)])
    dqkv = jnp.concatenate(dqs + dks + dvs + [dqp, dkp, dvp], axis=1)

    gw_qkv = _mm(h, dqkv, name="gw_qkv", ta=True, out_dtype=_CD, tn=768, tk=S)
    gw_g = _mm(h, dgl, name="gw_gate", ta=True, out_dtype=_CD, tk=S)
    gw_f = _mm(h, dzf, name="gw_f", ta=True, out_dtype=_CD)
    tok = sink("in", dict(qkv=gw_qkv, f=gw_f, g=gw_g))
    dx, dg_mix = _mm(dqkv, w["qkv"], name="d_h", tb=True, tk=QKV_COLS, tm=256, more=((dgl, w["g"]), (dzf, w["f"])),
                     rms_bwd=(x, p["norm_mix_g"], dx1), after=tok)

    gw = dict(qkv=gw_qkv, f=gw_f, g=gw_g, dil_out=gw_dil_out, fox_out=gw_fox_out, out=gw_out, ffn_in=gw_ffn_in,
              ffn_down=gw_ffn_down)
    small = dict(norm_mix_g=dg_mix, b_fgt=db_fgt, b_gate=db_gate, norm_ffn_g=dg_ffn, norm_final_g=dg_final)
    return loss, dx, gw, small


def _position():
    return lax.axis_index("x"), lax.axis_index("y"), lax.axis_index("c")


def _other_chips(x, y):
    return [(1 - x, y), (x, 1 - y), (1 - x, 1 - y)]


ROW_TILE = 16


def _row_chunks(rows, want=4):
    n = want
    while n > 1 and rows % (n * ROW_TILE):
        n //= 2
    return n


SEM_SPEC = pl.BlockSpec(memory_space=pltpu.SEMAPHORE)
ANY_SPEC = pl.BlockSpec(memory_space=pl.ANY)
DATAFLOW = pltpu.SideEffectType.DATAFLOW_SIDE_EFFECTING


def _in_hbm(a):
    return pltpu.with_memory_space_constraint(a, pltpu.HBM)


def _split_copy_start(srcs, land_shapes, copies, after, *, name):
    n, m = len(srcs), len(land_shapes)

    def body(*refs):
        src_refs, land_refs = refs[:n], refs[n:n + m]
        send_sems, recv_sems = refs[n + m + 1], refs[n + m + 2]
        token = refs[-1]
        x, y, c = _position()
        for k, (src, dst, peer) in enumerate(copies(x, y, c, src_refs, land_refs)):
            pltpu.make_async_remote_copy(src_ref=src, dst_ref=dst, send_sem=send_sems.at[k], recv_sem=recv_sems.at[k],
                                         device_id=peer, device_id_type=MESH).start()
        token[...] = jnp.zeros_like(token)

    lands = [lax.empty(s.shape, s.dtype) for s in land_shapes]
    count = len(copies(0, 0, 0, srcs, lands))
    out = _pcall(
        body, name=name,
        out_shape=(pltpu.SemaphoreType.DMA((count,)), pltpu.SemaphoreType.DMA((count,)),
                   *[pltpu.HBM(s.shape, s.dtype) for s in srcs], *[pltpu.HBM(s.shape, s.dtype) for s in land_shapes],
                   jax.ShapeDtypeStruct((8, 128), F32)),
        in_specs=[HBM_SPEC] * (n + m) + [ANY_SPEC],
        out_specs=(SEM_SPEC, SEM_SPEC, *[HBM_SPEC] * (n + m), pl.BlockSpec(memory_space=pltpu.VMEM)),
        input_output_aliases={k: 2 + k for k in range(n + m)},
        compiler_params=pltpu.CompilerParams(has_side_effects=DATAFLOW),
    )(*[_in_hbm(s) for s in srcs], *[_in_hbm(l) for l in lands], after)
    return out[0], out[1], list(out[2:2 + n]), list(out[2 + n:2 + n + m]), out[-1]


def _split_copy_wait(send_sems, recv_sems, srcs, lands, copies, after, *, name):
    n, m = len(srcs), len(lands)

    def body(*refs):
        src_refs, land_refs = refs[:n], refs[n:n + m]
        send, recv = refs[n + m], refs[n + m + 1]
        x, y, c = _position()
        for k, (src, dst, peer) in enumerate(copies(x, y, c, src_refs, land_refs)):
            cp = pltpu.make_async_remote_copy(src_ref=src, dst_ref=dst, send_sem=send.at[k], recv_sem=recv.at[k],
                                              device_id=peer, device_id_type=MESH)
            cp.wait_send()
            cp.wait_recv()

    afters = list(after) if isinstance(after, (list, tuple)) else [after]
    out = _pcall(
        body, name=name,
        out_shape=tuple(pltpu.HBM(s.shape, s.dtype) for s in list(srcs) + list(lands)),
        in_specs=[HBM_SPEC] * (n + m) + [SEM_SPEC, SEM_SPEC] + [ANY_SPEC] * len(afters),
        out_specs=tuple([HBM_SPEC] * (n + m)),
        input_output_aliases={k: k for k in range(n + m)},
        compiler_params=pltpu.CompilerParams(has_side_effects=DATAFLOW),
    )(*srcs, *lands, send_sems, recv_sems, *afters)
    return list(out[:n]), list(out[n:])


def _gather_copies(x, y, c, shard_refs, land_refs):
    out = []
    for s, l in zip(shard_refs, land_refs):
        half = s.shape[0] // 2
        nq = _row_chunks(half)
        for cx, cy in _other_chips(x, y):
            for q in range(nq):
                rows = pl.ds(c * half + q * (half // nq), half // nq)
                out.append((s.at[rows, :], l.at[2 * x + y, rows, :], (cx, cy, c)))
    return out


def _gather_whole_copies(x, y, c, shard_refs, land_refs):
    out = []
    for s, l in zip(shard_refs, land_refs):
        nq = _row_chunks(s.shape[0])
        for cx, cy in _other_chips(x, y):
            for q in range(nq):
                rows = pl.ds(q * (s.shape[0] // nq), s.shape[0] // nq)
                out.append((s.at[rows, :], l.at[2 * x + y, rows, :], (cx, cy, c)))
    return out


def _scatter_all_copies(x, y, c, block_refs, land_refs):
    out = []
    for g, l in zip(block_refs, land_refs):
        half = g.shape[1] // 2
        nq = _row_chunks(half)
        size = half // nq
        for q in range(nq):
            rows = pl.ds((1 - c) * half + q * size, size)
            out.append((g.at[2 * x + y, rows, :], l.at[0, pl.ds(q * size, size), :], (x, y, 1 - c)))
        for r, (cx, cy) in enumerate(_other_chips(x, y)):
            for j in range(2):
                h = c if j == 0 else 1 - c
                for q in range(nq):
                    rows = pl.ds(h * half + q * size, size)
                    out.append((g.at[2 * cx + cy, rows, :], l.at[1 + 2 * r + j, pl.ds(q * size, size), :], (cx, cy, h)))
    return out


def _forward_halves(lands, *, name):
    n = len(lands)

    def body(*refs):
        ins = refs[:n]
        send_sems, recv_sems = refs[2 * n:]
        x, y, c = _position()
        copies = []
        for w in range(n):
            half = ins[w].shape[1] // 2
            for r, (cx, cy) in enumerate(_other_chips(x, y)):
                blk = ins[w].at[2 * cx + cy, pl.ds(c * half, half), :]
                cp = pltpu.make_async_remote_copy(src_ref=blk, dst_ref=blk, send_sem=send_sems.at[w, r],
                                                  recv_sem=recv_sems.at[w, r], device_id=(x, y, 1 - c),
                                                  device_id_type=MESH)
                cp.start()
                copies.append(cp)
        for w in range(n):
            half = ins[w].shape[1] // 2
            for r, (cx, cy) in enumerate(_other_chips(x, y)):
                blk = ins[w].at[2 * cx + cy, pl.ds((1 - c) * half, half), :]
                pltpu.make_async_remote_copy(src_ref=blk, dst_ref=blk, send_sem=send_sems.at[w, r],
                                             recv_sem=recv_sems.at[w, r], device_id=(x, y, 1 - c),
                                             device_id_type=MESH).wait_recv()
        for cp in copies:
            cp.wait_send()

    return _pcall(
        body, name=name, in_specs=[HBM_SPEC] * n, out_specs=[HBM_SPEC] * n,
        out_shape=[jax.ShapeDtypeStruct(l.shape, l.dtype) for l in lands],
        input_output_aliases={k: k for k in range(n)},
        scratch_shapes=[pltpu.SemaphoreType.DMA((n, 3)), pltpu.SemaphoreType.DMA((n, 3))],
    )(*lands)


def _share_halves(halves, *, name):
    n = len(halves)

    def body(*refs):
        ins, outs = refs[:n], refs[n:2 * n]
        send_sems, recv_sems = refs[2 * n:]
        x, y, c = _position()
        copies = []
        for w in range(n):
            cp = pltpu.make_async_remote_copy(src_ref=ins[w], dst_ref=outs[w], send_sem=send_sems.at[w],
                                              recv_sem=recv_sems.at[w], device_id=(x, y, 1 - c), device_id_type=MESH)
            cp.start()
            copies.append(cp)
        for cp in copies:
            cp.wait()

    return _pcall(
        body, name=name, in_specs=[HBM_SPEC] * n, out_specs=[HBM_SPEC] * n,
        out_shape=[jax.ShapeDtypeStruct(h.shape, h.dtype) for h in halves],
        scratch_shapes=[pltpu.SemaphoreType.DMA((n,)), pltpu.SemaphoreType.DMA((n,))],
    )(*halves)


def _sum_small(part, after=None):
    rows, width = part.shape

    def body(x_ref, out_ref, all_ref, send_sems, recv_sems):
        x, y, c = _position()
        me, sibling = (x, y, c), (x, y, 1 - c)
        chips = _other_chips(x, y)

        def block(px, py, pc):
            return all_ref.at[pl.ds((4 * px + 2 * py + pc) * rows, rows), :]

        def copy(k, blk, to, src=None):
            return pltpu.make_async_remote_copy(
                src_ref=block(*blk) if src is None else src, dst_ref=block(*blk), send_sem=send_sems.at[k],
                recv_sem=recv_sems.at[k], device_id=to, device_id_type=MESH)

        all_ref[pl.ds((4 * x + 2 * y + c) * rows, rows), :] = x_ref[...]
        first = [copy(0, me, sibling, src=x_ref)]
        first += [copy(1 + j, me, (*chip, c), src=x_ref) for j, chip in enumerate(chips)]
        for cp in first:
            cp.start()
        passed = [copy(4 + j, (*chip, c), sibling) for j, chip in enumerate(chips)]
        for j, chip in enumerate(chips):
            copy(1 + j, (*chip, c), me).wait_recv()
            passed[j].start()
        copy(0, sibling, me).wait_recv()
        for j, chip in enumerate(chips):
            copy(4 + j, (*chip, 1 - c), me).wait_recv()
        for cp in first + passed:
            cp.wait_send()
        total = all_ref[0:rows, :]
        for d in range(1, 8):
            total = total + all_ref[d * rows:(d + 1) * rows, :]
        out_ref[...] = total

    vm = pl.BlockSpec(memory_space=pltpu.VMEM)
    return _pcall(
        body, after, name="sum_small", in_specs=[vm], out_specs=vm, out_shape=jax.ShapeDtypeStruct((rows, width), F32),
        scratch_shapes=[pltpu.VMEM((8 * rows, width), F32), pltpu.SemaphoreType.DMA((7,)), pltpu.SemaphoreType.DMA((7,))],
    )(part)


def _row_tile(R, C, itemsize=4, budget=1 << 20):
    fits = [t for t in range(ROW_TILE, R + 1, ROW_TILE) if R % t == 0 and t * C * itemsize <= budget]
    return max(fits) if fits else R


def _add_all(g, recv, where, *, name):
    _, R, C = g.shape
    half = R // 2
    t = _row_tile(half, C)
    nb = half // t

    def body(w_ref, g_ref, r_ref, o_ref):
        total = g_ref[0].astype(F32)
        for k in range(7):
            total = total + r_ref[k].astype(F32)
        o_ref[...] = total

    grid_spec = pltpu.PrefetchScalarGridSpec(
        num_scalar_prefetch=1, grid=(nb,),
        in_specs=[pl.BlockSpec((1, t, C), lambda i, wr: (wr[0], wr[1] * nb + i, 0)),
                  pl.BlockSpec((7, t, C), lambda i, wr: (0, i, 0))],
        out_specs=pl.BlockSpec((t, C), lambda i, wr: (i, 0)))
    return _pcall(body, name=name, grid_spec=grid_spec, out_shape=jax.ShapeDtypeStruct((half, C), F32),
                  compiler_params=_params("parallel"))(where, g, recv)


def _adamw(w, g, m, v, *, name):
    R, C = w.shape
    t = _row_tile(R, C)
    c1 = 1.0 - ADAM_B1 ** ADAM_STEP
    c2 = 1.0 - ADAM_B2 ** ADAM_STEP

    def body(w_ref, g_ref, m_ref, v_ref, d_ref, nm_ref, nv_ref):
        gv = g_ref[...]
        mn = ADAM_B1 * m_ref[...] + (1.0 - ADAM_B1) * gv
        vn = ADAM_B2 * v_ref[...] + (1.0 - ADAM_B2) * (gv * gv)
        d_ref[...] = -ADAM_LR * ((mn / c1) / (jnp.sqrt(vn / c2) + ADAM_EPS) + ADAM_WD * w_ref[...])
        nm_ref[...] = mn
        nv_ref[...] = vn

    blk = pl.BlockSpec((t, C), lambda i: (i, 0))
    shp = jax.ShapeDtypeStruct((R, C), F32)
    return _pcall(body, name=name, grid=(R // t,), in_specs=[blk] * 4, out_specs=[blk] * 3, out_shape=[shp] * 3,
                  compiler_params=_params("parallel"))(w, g, m, v)


def _adamw_halves(w, mine, theirs, m, v, core, *, name, after=None):
    R, C = w.shape
    half = R // 2
    t = _row_tile(half, C)
    nbh = half // t
    c1 = 1.0 - ADAM_B1 ** ADAM_STEP
    c2 = 1.0 - ADAM_B2 ** ADAM_STEP

    def body(core_ref, w_ref, a_ref, b_ref, m_ref, v_ref, *rest):
        g_ref, d_ref, nm_ref, nv_ref = rest[-4:]
        gv = jnp.where(pl.program_id(0) // nbh == core_ref[0], a_ref[...], b_ref[...])
        mn = ADAM_B1 * m_ref[...] + (1.0 - ADAM_B1) * gv
        vn = ADAM_B2 * v_ref[...] + (1.0 - ADAM_B2) * (gv * gv)
        g_ref[...] = gv
        d_ref[...] = -ADAM_LR * ((mn / c1) / (jnp.sqrt(vn / c2) + ADAM_EPS) + ADAM_WD * w_ref[...])
        nm_ref[...] = mn
        nv_ref[...] = vn

    blk = pl.BlockSpec((t, C), lambda i, cr: (i, 0))
    hblk = pl.BlockSpec((t, C), lambda i, cr: (i % nbh, 0))
    shp = jax.ShapeDtypeStruct((R, C), F32)
    tied = [] if after is None else [after]
    grid_spec = pltpu.PrefetchScalarGridSpec(num_scalar_prefetch=1, grid=(2 * nbh,),
                                             in_specs=[blk, hblk, hblk, blk, blk] + [ANY_SPEC] * len(tied),
                                             out_specs=[blk] * 4)
    return _pcall(body, name=name, grid_spec=grid_spec, out_shape=[shp] * 4,
                  compiler_params=_params("parallel"))(core, w, mine, theirs, m, v, *tied)


BIG = ("w_in", "w_dil_out", "w_fox_out", "w_out", "w_ffn_in", "w_ffn_down")
SMALL = ("norm_mix_g", "b_fgt", "b_gate", "norm_ffn_g", "norm_final_g")
ORDER = ("norm_mix_g", "w_in", "b_fgt", "b_gate", "w_dil_out", "w_fox_out", "w_out", "norm_ffn_g", "w_ffn_in",
         "w_ffn_down", "norm_final_g")
SMALL_ROWS = {"norm_mix_g": (0, 1), "b_gate": (1, 3), "norm_ffn_g": (3, 4), "norm_final_g": (4, 5), "b_fgt": (5, 6)}


def _columns_to_blocks(full, ncol):
    K = full.shape[0]
    return full.reshape(K, 4, ncol).transpose(1, 0, 2)


def _pieces_to_blocks(pieces, ncol):
    spans, start = [], 0
    for piece in pieces:
        spans.append((piece, start, start + piece.shape[1]))
        start += piece.shape[1]
    assert start == 4 * ncol
    blocks = []
    for k in range(4):
        lo, hi = k * ncol, (k + 1) * ncol
        parts = [p[:, max(lo, a) - a:min(hi, b) - a] for p, a, b in spans if a < hi and b > lo]
        blocks.append(parts[0] if len(parts) == 1 else jnp.concatenate(parts, axis=1))
    return jnp.stack(blocks)


def _blocks_to_pieces(blocks, widths):
    n, K, ncol = blocks.shape
    assert sum(widths) == n * ncol
    pieces, lo = [], 0
    for width in widths:
        hi = lo + width
        parts = [blocks[k][:, max(lo, k * ncol) - k * ncol:min(hi, (k + 1) * ncol) - k * ncol]
                 for k in range(n) if k * ncol < hi and (k + 1) * ncol > lo]
        pieces.append(parts[0] if len(parts) == 1 else jnp.concatenate(parts, axis=1))
        lo = hi
    return pieces


def _blocks_to_columns(blocks):
    n, K, ncol = blocks.shape
    return blocks.transpose(1, 0, 2).reshape(K, n * ncol)


def kernel(x, norm_mix_g, w_in, b_fgt, b_gate, w_dil_out, w_fox_out, w_out, norm_ffn_g, w_ffn_in, w_ffn_down, norm_final_g, loss_target, m_norm_mix_g, m_w_in, m_b_fgt, m_b_gate, m_w_dil_out, m_w_fox_out, m_w_out, m_norm_ffn_g, m_w_ffn_in, m_w_ffn_down, m_norm_final_g, v_norm_mix_g, v_w_in, v_b_fgt, v_b_gate, v_w_dil_out, v_w_fox_out, v_w_out, v_norm_ffn_g, v_w_ffn_in, v_w_ffn_down, v_norm_final_g):
    weights = dict(norm_mix_g=norm_mix_g, w_in=w_in, b_fgt=b_fgt, b_gate=b_gate, w_dil_out=w_dil_out,
                   w_fox_out=w_fox_out, w_out=w_out, norm_ffn_g=norm_ffn_g, w_ffn_in=w_ffn_in, w_ffn_down=w_ffn_down,
                   norm_final_g=norm_final_g)
    m_in = dict(norm_mix_g=m_norm_mix_g, w_in=m_w_in, b_fgt=m_b_fgt, b_gate=m_b_gate, w_dil_out=m_w_dil_out,
                w_fox_out=m_w_fox_out, w_out=m_w_out, norm_ffn_g=m_norm_ffn_g, w_ffn_in=m_w_ffn_in,
                w_ffn_down=m_w_ffn_down, norm_final_g=m_norm_final_g)
    v_in = dict(norm_mix_g=v_norm_mix_g, w_in=v_w_in, b_fgt=v_b_fgt, b_gate=v_b_gate, w_dil_out=v_w_dil_out,
                w_fox_out=v_w_fox_out, w_out=v_w_out, norm_ffn_g=v_norm_ffn_g, w_ffn_in=v_w_ffn_in,
                w_ffn_down=v_w_ffn_down, norm_final_g=v_norm_final_g)
    c = lax.axis_index("c")
    chip = 2 * lax.axis_index("x") + lax.axis_index("y")

    shards = {n: weights[n][0].astype(_CD) for n in BIG}
    in_shape = jax.ShapeDtypeStruct((4,) + shards["w_in"].shape, _CD)
    send_i, recv_i, in_src, in_land, token_in = _split_copy_start(
        [shards["w_in"]], [in_shape], _gather_copies, norm_mix_g, name="gather_in_start")
    late = BIG[1:]
    send_g, recv_g, late_src, late_land, token = _split_copy_start(
        [shards[n] for n in late], [jax.ShapeDtypeStruct((4,) + shards[n].shape, _CD) for n in late],
        _gather_whole_copies, token_in, name="gather_late_start")
    adam_in = [t[0] + token_in[0, 0] for t in (w_in, m_w_in, v_w_in)]
    p = dict(norm_mix_g=norm_mix_g, b_fgt=jnp.pad(b_fgt, ((0, 0), (0, F_PAD - N_FOX_HEADS))), b_gate=b_gate,
             norm_ffn_g=norm_ffn_g, norm_final_g=norm_final_g.reshape(1, D_MODEL))

    def first_weights(after):
        own, lands = _split_copy_wait(send_i, recv_i, in_src, in_land, _gather_copies, [after] + adam_in,
                                      name="gather_in_wait")
        (g_in,) = _forward_halves(lands, name="gather_in_forward")
        blocks = lax.dynamic_update_index_in_dim(g_in, own[0], chip, 0)
        qkv, f, g = _blocks_to_pieces(blocks, (QKV_COLS, N_FOX_HEADS, 2 * D_MODEL))
        return dict(qkv=qkv, f=jnp.pad(f, ((0, 0), (0, F_PAD - N_FOX_HEADS))), g=g)

    def late_weights(after):
        own, lands = _split_copy_wait(send_g, recv_g, late_src, late_land, _gather_whole_copies, after,
                                      name="gather_late_wait")
        g_dil, g_fox, g_out, g_ffn_in, g_ffn_down = [
            lax.dynamic_update_index_in_dim(l, s, chip, 0) for l, s in zip(lands, own)]
        return dict(dil_out=_blocks_to_columns(g_dil), fox_out=_blocks_to_columns(g_fox),
                    out=g_out.reshape(D_MODEL, D_MODEL), ffn_in=g_ffn_in,
                    ffn_down=g_ffn_down.reshape(D_FF, D_MODEL))

    def to_blocks(n, full):
        shape = weights[n].shape
        if full.ndim == 3:
            return full
        if n in ("w_out", "w_ffn_down"):
            return full.reshape(4, shape[1], shape[2])
        return _columns_to_blocks(full, shape[2])

    in_flight = {}

    def grad_sink(group, gw):
        if group == "in":
            named = {"w_in": _pieces_to_blocks([gw["qkv"], gw["f"][:, :N_FOX_HEADS], gw["g"]], weights["w_in"].shape[2])}
        else:
            named = {"w_" + k: v for k, v in gw.items()}
        srcs = [to_blocks(n, named[n]) for n in named]
        lands = [jax.ShapeDtypeStruct((7, s.shape[1] // 2, s.shape[2]), s.dtype) for s in srcs]
        started = _split_copy_start(srcs, lands, _scatter_all_copies, next(iter(gw.values())),
                                    name=f"scatter_{group}_start")
        in_flight[group] = (list(named), started)
        return started[-1]

    loss_part, grad_x, gw, small = _layer_step(x[0], loss_target[0], {}, p, late_weights, grad_sink,
                                               (token_in, token), first_weights)

    where = jnp.stack([chip, c]).astype(jnp.int32)

    def summed_halves(groups, after, name):
        halves = {}
        for group in groups:
            names, (send_s, recv_s, srcs, lands, _) = in_flight[group]
            srcs, recv = _split_copy_wait(send_s, recv_s, srcs, lands, _scatter_all_copies, after,
                                          name=f"scatter_{group}_wait")
            halves.update({n: _add_all(s, r, where, name=f"add_all_{n}") for n, s, r in zip(names, srcs, recv)})
        return {n: (h, o) for (n, h), o in zip(halves.items(), _share_halves(list(halves.values()), name=name))}

    grad_halves = summed_halves(("ffn", "mix"), grad_x, "share_halves")

    out_g, out_d, out_m, out_v = {}, {}, {}, {}
    core = jnp.reshape(c, (1,)).astype(jnp.int32)

    def adamw_big(n, after=None):
        shape = weights[n].shape
        wmv = adam_in if n == "w_in" else [t[0] for t in (weights[n], m_in[n], v_in[n])]
        mine, theirs = grad_halves[n]
        outs = _adamw_halves(wmv[0], mine, theirs, wmv[1], wmv[2], core, name=f"adamw_{n}", after=after)
        out_g[n], out_d[n], out_m[n], out_v[n] = [t.reshape(shape) for t in outs]

    early = ("w_dil_out", "w_fox_out", "w_out", "w_ffn_down", "w_ffn_in")
    for n in early:
        adamw_big(n)
    grad_halves.update(summed_halves(("in",), [grad_x] + [out_d[n] for n in early], "share_halves_in"))
    adamw_big("w_in")

    packed = jnp.concatenate([
        small["norm_mix_g"], small["b_gate"].reshape(2, D_MODEL), small["norm_ffn_g"], small["norm_final_g"],
        jnp.pad(small["b_fgt"], ((0, 0), (0, D_MODEL - F_PAD))), jnp.pad(loss_part, ((0, 0), (0, D_MODEL - 1))),
        jnp.zeros((1, D_MODEL), F32)], axis=0)
    summed = _sum_small(packed, after=out_d["w_in"])
    loss = summed[6, 0]

    for n in SMALL:
        lo, hi = SMALL_ROWS[n]
        shape = weights[n].shape
        g2 = summed[lo:hi].reshape(1, -1)[:, :weights[n].size]
        d2, m2, v2 = _adamw(weights[n].reshape(g2.shape), g2, m_in[n].reshape(g2.shape), v_in[n].reshape(g2.shape),
                            name=f"adamw_{n}")
        out_g[n], out_d[n], out_m[n], out_v[n] = [t.reshape(shape) for t in (g2, d2, m2, v2)]
    return (loss, grad_x[None], *[out_g[n] for n in ORDER], *[out_d[n] for n in ORDER],
            *[out_m[n] for n in ORDER], *[out_v[n] for n in ORDER])
```

```python
import numpy as np
import jax
import jax.numpy as jnp
from jax import lax
from jax.experimental import pallas as pl
from jax.experimental.pallas import tpu as pltpu

F32 = jnp.float32
_CD = jnp.bfloat16

D_MODEL = 1024
HEAD_DIM = 64
DIL_PAIRS = ((128, 1), (512, 4), (2048, 16))
N_DIL_GROUPS = 3
DIL_HEADS = 4
DIL_W = 128
DIL_OUT = DIL_HEADS * HEAD_DIM
DIL_WIDTH = N_DIL_GROUPS * DIL_OUT
N_FOX_HEADS = 8
FOX_WIDTH = N_FOX_HEADS * HEAD_DIM
D_FF = 2816
QKV_COLS = 3 * DIL_WIDTH + 3 * FOX_WIDTH
F_PAD = 128
RMS_EPS = 1e-6
NEG_INF = -1e30
ATTN_SCALE = HEAD_DIM ** -0.5
ADAM_LR, ADAM_B1, ADAM_B2, ADAM_EPS, ADAM_WD, ADAM_STEP = 0.001, 0.9, 0.999, 1e-08, 0.01, 10

VMEM_LIMIT = 48 * 1024 * 1024
VMEM_LIMIT_RESIDENT = 56 * 1024 * 1024
LANES = 128
MESH = pl.DeviceIdType.MESH
HBM_SPEC = pl.BlockSpec(memory_space=pltpu.HBM)


def _pcall(body, after=None, **kw):
    if after is None:
        return pl.pallas_call(body, **kw)
    n_in = len(kw["in_specs"])
    kw["in_specs"] = list(kw["in_specs"]) + [pl.BlockSpec(memory_space=pl.ANY)]

    def tied(*refs):
        return body(*refs[:n_in], *refs[n_in + 1:])

    call = pl.pallas_call(tied, **kw)
    return lambda *args: call(*args, after)


def _params(*sem):
    return pltpu.CompilerParams(dimension_semantics=sem, vmem_limit_bytes=VMEM_LIMIT)


def _pick(dim, pref):
    t = (min(pref, dim) // 128) * 128
    while t >= 128:
        if dim % t == 0:
            return t
        t -= 128
    return dim


def _rms_bwd_store(r, x_ref, g_ref, dres_ref, o_ref, dg_ref):
    xv = x_ref[...]
    rs = lax.rsqrt(jnp.mean(xv * xv, axis=-1, keepdims=True) + RMS_EPS)
    xh = xv * rs
    dxh = r * g_ref[...]
    o_ref[...] = dres_ref[...] + rs * (dxh - xh * jnp.mean(dxh * xh, axis=-1, keepdims=True))
    part = jnp.sum(r * xh, axis=0, keepdims=True)
    first = pl.program_id(0) == 0

    @pl.when(first)
    def _():
        dg_ref[...] = part

    @pl.when(jnp.logical_not(first))
    def _():
        dg_ref[...] += part


def _d_h2(dgu, w_blocks, x, g, dres, *, name, tm=256, after=None):
    _, S, F = dgu.shape
    n, D, C = w_blocks.shape
    assert n == 4 and F == 2 * C
    nt = (((1,), (1,)), ((), ()))

    def body(dg_ref, du_ref, w_ref, x_ref, g_ref, dres_ref, o_ref, dgn_ref):
        r = None
        for k in range(n):
            a_ref = dg_ref if k < 2 else du_ref
            p = lax.dot_general(a_ref[:, (k % 2) * C:(k % 2 + 1) * C].astype(_CD), w_ref[k].astype(_CD), nt,
                                preferred_element_type=F32)
            r = p if r is None else r + p
        _rms_bwd_store(r, x_ref, g_ref, dres_ref, o_ref, dgn_ref)

    row = pl.BlockSpec((tm, D), lambda i: (i, 0))
    vec = pl.BlockSpec((1, D), lambda i: (0, 0))
    return _pcall(
        body, after, name=name, grid=(S // tm,),
        in_specs=[pl.BlockSpec((None, tm, F), lambda i: (0, i, 0)), pl.BlockSpec((None, tm, F), lambda i: (1, i, 0)),
                  pl.BlockSpec((n, D, C), lambda i: (0, 0, 0)), row, vec, row],
        out_specs=[row, vec], out_shape=[jax.ShapeDtypeStruct((S, D), F32), jax.ShapeDtypeStruct((1, D), F32)],
        compiler_params=_params("arbitrary"))(dgu, dgu, w_blocks, x, g, dres)


def _mm(a, b, *, name, ta=False, tb=False, out_dtype=F32, tm=1024, tn=512, tk=2048, after=None,
        out_blocks=None, b_halves=False, rms_bwd=None, more=()):
    K, M = a.shape if ta else a.shape[::-1]
    b_rows, b_cols = (b.shape[1], 2 * b.shape[2]) if b_halves else b.shape
    if tb:
        N, K2 = b_rows, b_cols
    else:
        K2, N = b_rows, b_cols
    assert K == K2, (a.shape, b.shape)
    tm = _pick(M, tm)
    tn = _pick(out_blocks or N, tn)
    tk = _pick(K, tk)
    nk = K // tk
    dn = (((0 if ta else 1,), (1 if tb else 0,)), ((), ()))
    has_norm = rms_bwd is not None
    if has_norm:
        tn = N
        assert not out_blocks and out_dtype == F32
    assert not more or (nk == 1 and tb and not ta)

    def body(*refs):
        a_ref, b_ref = refs[0], refs[1]
        rest = list(refs[2:])
        more_refs = [(rest.pop(0), rest.pop(0)) for _ in more]
        x_ref, g_ref, dres_ref = (rest.pop(0), rest.pop(0), rest.pop(0)) if has_norm else (None, None, None)
        o_ref = rest.pop(0)
        dg_ref = rest.pop(0) if has_norm else None
        p = lax.dot_general(a_ref[...].astype(_CD), b_ref[...].astype(_CD), dn, preferred_element_type=F32)
        for a2_ref, b2_ref in more_refs:
            p += lax.dot_general(a2_ref[...].astype(_CD), b2_ref[...].astype(_CD), dn, preferred_element_type=F32)

        def finish(r):
            if has_norm:
                _rms_bwd_store(r, x_ref, g_ref, dres_ref, o_ref, dg_ref)
            elif out_blocks:
                o_ref[0] = r.astype(out_dtype)
            else:
                o_ref[...] = r.astype(out_dtype)

        if nk == 1:
            finish(p)
        else:
            acc_ref = rest.pop(0)
            k = pl.program_id(2)

            @pl.when(k == 0)
            def _():
                acc_ref[...] = p

            @pl.when(k > 0)
            def _():
                acc_ref[...] += p

            @pl.when(k == nk - 1)
            def _():
                finish(acc_ref[...])

    a_spec = pl.BlockSpec((tk, tm), lambda i, j, k: (k, i)) if ta else pl.BlockSpec((tm, tk), lambda i, j, k: (i, k))
    if b_halves:
        nb_ = (N // 2) // tn
        b_spec = pl.BlockSpec((None, tk, tn), lambda i, j, k: (j // nb_, k, j % nb_))
    else:
        b_spec = pl.BlockSpec((tn, tk), lambda i, j, k: (j, k)) if tb else pl.BlockSpec((tk, tn), lambda i, j, k: (k, j))
    if out_blocks:
        oper = out_blocks // tn
        o_spec = pl.BlockSpec((1, tm, tn), lambda i, j, k: (j // oper, i, j % oper))
        out_shape = jax.ShapeDtypeStruct((N // out_blocks, M, out_blocks), out_dtype)
    else:
        o_spec = pl.BlockSpec((tm, tn), lambda i, j, k: (i, j))
        out_shape = jax.ShapeDtypeStruct((M, N), out_dtype)
    in_specs, args = [a_spec, b_spec], (a, b)
    for a2, b2 in more:
        assert a2.shape[0] == M and b2.shape == (N, a2.shape[1]), (a2.shape, b2.shape)
        in_specs += [pl.BlockSpec((tm, a2.shape[1]), lambda i, j, k: (i, 0)),
                     pl.BlockSpec((tn, a2.shape[1]), lambda i, j, k: (j, 0))]
        args += (a2, b2)
    out_specs, semantics = o_spec, ("parallel", "parallel", "arbitrary")
    if has_norm:
        vec = pl.BlockSpec((1, N), lambda i, j, k: (0, 0))
        in_specs += [o_spec, vec, o_spec]
        args += tuple(rms_bwd)
        out_specs, out_shape = [o_spec, vec], [out_shape, jax.ShapeDtypeStruct((1, N), F32)]
        semantics = ("arbitrary", "arbitrary", "arbitrary")
    return _pcall(
        body, after, name=name, grid=(M // tm, N // tn, nk), in_specs=in_specs, out_specs=out_specs,
        out_shape=out_shape,
        scratch_shapes=[pltpu.VMEM((tm, tn), F32)] if nk > 1 else [],
        compiler_params=_params(*semantics),
    )(*args)


def _rms_fwd(x, g, *, name, tm=512, after=None):
    S, D = x.shape

    def body(x_ref, g_ref, h_ref):
        xv = x_ref[...]
        r = lax.rsqrt(jnp.mean(xv * xv, axis=-1, keepdims=True) + RMS_EPS)
        h_ref[...] = ((xv * r) * g_ref[...]).astype(h_ref.dtype)

    row = pl.BlockSpec((tm, D), lambda i: (i, 0))
    return _pcall(body, after, name=name, grid=(S // tm,), in_specs=[row, pl.BlockSpec((1, D), lambda i: (0, 0))],
                  out_specs=row, out_shape=jax.ShapeDtypeStruct((S, D), _CD), compiler_params=_params("parallel"))(x, g)


def _ffn_down_loss(act, w_down, x1, g, tgt, *, name, tm=512):
    S, D = x1.shape
    F = act.shape[1]

    def body(a_ref, b_ref, x_ref, g_ref, t_ref, loss_ref, dx_ref, dg_ref):
        xv = x_ref[...] + jnp.dot(a_ref[...].astype(_CD), b_ref[...].astype(_CD), preferred_element_type=F32)
        gv = g_ref[...]
        r = lax.rsqrt(jnp.mean(xv * xv, axis=-1, keepdims=True) + RMS_EPS)
        xh = xv * r
        err = xh * gv - t_ref[...]
        lpart = 0.5 * jnp.sum(jnp.mean(err * err, axis=-1, keepdims=True), axis=0, keepdims=True)
        dy = err * (1.0 / D)
        dxh = dy * gv
        dx_ref[...] = r * (dxh - xh * jnp.mean(dxh * xh, axis=-1, keepdims=True))
        gpart = jnp.sum(dy * xh, axis=0, keepdims=True)

        @pl.when(pl.program_id(0) == 0)
        def _():
            loss_ref[...] = lpart
            dg_ref[...] = gpart

        @pl.when(pl.program_id(0) > 0)
        def _():
            loss_ref[...] += lpart
            dg_ref[...] += gpart

    row = pl.BlockSpec((tm, D), lambda i: (i, 0))
    vec = pl.BlockSpec((1, D), lambda i: (0, 0))
    one = pl.BlockSpec((1, 1), lambda i: (0, 0))
    return _pcall(body, name=name, grid=(S // tm,),
                  in_specs=[pl.BlockSpec((tm, F), lambda i: (i, 0)), pl.BlockSpec((F, D), lambda i: (0, 0)), row, vec, row],
                  out_specs=[one, row, vec],
                  out_shape=[jax.ShapeDtypeStruct((1, 1), F32), jax.ShapeDtypeStruct((S, D), F32),
                             jax.ShapeDtypeStruct((1, D), F32)],
                  compiler_params=_params("arbitrary"))(act, w_down, x1, g, tgt)


def _sigmoid(z):
    return 1.0 / (1.0 + jnp.exp(-z))


def _gated_mix_out(gl, bg, ya, yb, w_out, x, g, *, name, tm=512):
    S, D = ya.shape

    def body(za_ref, zb_ref, ba_ref, bb_ref, ya_ref, yb_ref, w_ref, x_ref, g_ref, m_ref, x1_ref, h_ref):
        ga = _sigmoid(za_ref[...].astype(F32) + ba_ref[...])
        gb = _sigmoid(zb_ref[...].astype(F32) + bb_ref[...])
        merged = (ga * ya_ref[...].astype(F32) + gb * yb_ref[...].astype(F32)).astype(m_ref.dtype)
        m_ref[...] = merged
        x1 = x_ref[...] + jnp.dot(merged, w_ref[...].astype(_CD), preferred_element_type=F32)
        x1_ref[...] = x1
        rs = lax.rsqrt(jnp.mean(x1 * x1, axis=-1, keepdims=True) + RMS_EPS)
        h_ref[...] = ((x1 * rs) * g_ref[...]).astype(h_ref.dtype)

    lo = pl.BlockSpec((tm, D), lambda i: (i, 0))
    hi = pl.BlockSpec((tm, D), lambda i: (i, 1))
    vlo = pl.BlockSpec((1, D), lambda i: (0, 0))
    vhi = pl.BlockSpec((1, D), lambda i: (0, 1))
    whole = pl.BlockSpec((D, D), lambda i: (0, 0))
    return _pcall(body, name=name, grid=(S // tm,), in_specs=[lo, hi, vlo, vhi, lo, lo, whole, lo, vlo],
                  out_specs=[lo, lo, lo],
                  out_shape=[jax.ShapeDtypeStruct((S, D), _CD), jax.ShapeDtypeStruct((S, D), F32),
                             jax.ShapeDtypeStruct((S, D), _CD)],
                  compiler_params=_params("parallel"))(gl, gl, bg, bg, ya, yb, w_out, x, g)


def _gate_bwd(dx1, w_out, gl, bg, ya, yb, *, name, tm=512):
    S, D = ya.shape
    nt = (((1,), (1,)), ((), ()))

    def body(dx_ref, w_ref, za_ref, zb_ref, ba_ref, bb_ref, ya_ref, yb_ref, dya_ref, dyb_ref, dgl_ref, dbg_ref):
        dmv = lax.dot_general(dx_ref[...].astype(_CD), w_ref[...].astype(_CD), nt, preferred_element_type=F32)
        ga = _sigmoid(za_ref[...].astype(F32) + ba_ref[...])
        gb = _sigmoid(zb_ref[...].astype(F32) + bb_ref[...])
        dya_ref[...] = (dmv * ga).astype(dya_ref.dtype)
        dyb_ref[...] = (dmv * gb).astype(dyb_ref.dtype)
        dza = dmv * ya_ref[...].astype(F32) * ga * (1.0 - ga)
        dzb = dmv * yb_ref[...].astype(F32) * gb * (1.0 - gb)
        dgl_ref[:, :D] = dza.astype(dgl_ref.dtype)
        dgl_ref[:, D:] = dzb.astype(dgl_ref.dtype)
        pa = jnp.sum(dza, axis=0, keepdims=True)
        pb = jnp.sum(dzb, axis=0, keepdims=True)

        @pl.when(pl.program_id(0) == 0)
        def _():
            dbg_ref[:, :D] = pa
            dbg_ref[:, D:] = pb

        @pl.when(pl.program_id(0) > 0)
        def _():
            dbg_ref[:, :D] += pa
            dbg_ref[:, D:] += pb

    lo = pl.BlockSpec((tm, D), lambda i: (i, 0))
    hi = pl.BlockSpec((tm, D), lambda i: (i, 1))
    vlo = pl.BlockSpec((1, D), lambda i: (0, 0))
    vhi = pl.BlockSpec((1, D), lambda i: (0, 1))
    wide = pl.BlockSpec((tm, 2 * D), lambda i: (i, 0))
    vwide = pl.BlockSpec((1, 2 * D), lambda i: (0, 0))
    whole = pl.BlockSpec((D, D), lambda i: (0, 0))
    return _pcall(body, name=name, grid=(S // tm,), in_specs=[lo, whole, lo, hi, vlo, vhi, lo, lo],
                  out_specs=[lo, lo, wide, vwide],
                  out_shape=[jax.ShapeDtypeStruct((S, D), _CD), jax.ShapeDtypeStruct((S, D), _CD),
                             jax.ShapeDtypeStruct((S, 2 * D), _CD), jax.ShapeDtypeStruct((1, 2 * D), F32)],
                  compiler_params=_params("arbitrary"))(dx1, w_out, gl, gl, bg, bg, ya, yb)


def _ffn_in_act(h2, w_blocks, *, name, tm=512):
    S, D = h2.shape
    _, _, C = w_blocks.shape

    def body(a_ref, bg_ref, bu_ref, g_ref, u_ref, o_ref):
        av = a_ref[...].astype(_CD)
        gv = jnp.dot(av, bg_ref[0].astype(_CD), preferred_element_type=F32)
        uv = jnp.dot(av, bu_ref[0].astype(_CD), preferred_element_type=F32)
        g_ref[...] = gv.astype(g_ref.dtype)
        u_ref[...] = uv.astype(u_ref.dtype)
        o_ref[...] = (gv * _sigmoid(gv) * uv).astype(o_ref.dtype)

    out = pl.BlockSpec((tm, C), lambda i, j: (i, j))
    shp = jax.ShapeDtypeStruct((S, 2 * C), _CD)
    return _pcall(body, name=name, grid=(S // tm, 2),
                  in_specs=[pl.BlockSpec((tm, D), lambda i, j: (i, 0)), pl.BlockSpec((1, D, C), lambda i, j: (j, 0, 0)),
                            pl.BlockSpec((1, D, C), lambda i, j: (2 + j, 0, 0))],
                  out_specs=[out, out, out], out_shape=[shp, shp, shp],
                  compiler_params=_params("parallel", "arbitrary"))(h2, w_blocks, w_blocks)


def _d_swiglu(dx, w_down, gate, up, *, name, tm=512, tn=1408):
    S, D = dx.shape
    F = w_down.shape[0]
    nt = (((1,), (1,)), ((), ()))

    def body(a_ref, b_ref, g_ref, u_ref, o_ref):
        dv = lax.dot_general(a_ref[...].astype(_CD), b_ref[...].astype(_CD), nt, preferred_element_type=F32)
        gv = g_ref[...].astype(F32)
        sg = _sigmoid(gv)
        o_ref[0] = (dv * u_ref[...].astype(F32) * (sg * (1.0 + gv * (1.0 - sg)))).astype(o_ref.dtype)
        o_ref[1] = (dv * (gv * sg)).astype(o_ref.dtype)

    tile = pl.BlockSpec((tm, tn), lambda i, j: (i, j))
    return _pcall(body, name=name, grid=(S // tm, F // tn),
                  in_specs=[pl.BlockSpec((tm, D), lambda i, j: (i, 0)), pl.BlockSpec((tn, D), lambda i, j: (j, 0)),
                            tile, tile],
                  out_specs=pl.BlockSpec((2, tm, tn), lambda i, j: (0, i, j)),
                  out_shape=jax.ShapeDtypeStruct((2, S, F), _CD),
                  compiler_params=_params("parallel", "arbitrary"))(dx, w_down, gate, up)


def _split3(x):
    hi = x.astype(jnp.bfloat16)
    r1 = x - hi.astype(F32)
    mid = r1.astype(jnp.bfloat16)
    lo = (r1 - mid.astype(F32)).astype(jnp.bfloat16)
    return hi, mid, lo


def _ones_dot_left(ones, x):
    return sum(jnp.dot(ones, p, preferred_element_type=F32) for p in _split3(x))


def _ones_dot_right(x, ones):
    return sum(jnp.dot(p, ones, preferred_element_type=F32) for p in _split3(x))


def _head_sum(x):
    n = x.shape[1]
    r = lax.broadcasted_iota(jnp.int32, (n, n), 0) // HEAD_DIM
    c = lax.broadcasted_iota(jnp.int32, (n, n), 1) // HEAD_DIM
    return _ones_dot_right(x, (r == c).astype(jnp.bfloat16))


def _log_sigmoid(z):
    e = jnp.exp(-jnp.abs(z))
    t = 1.0 + e
    log1p_e = jnp.where(t == 1.0, e, jnp.log(t) * (e / jnp.where(t == 1.0, 1.0, t - 1.0)))
    return jnp.minimum(z, 0.0) - log1p_e


def _fox_cumsum(zf, bf, *, name):
    S, W = zf.shape
    nb = S // 128

    def body(z_ref, b_ref, c_ref):
        tri = (lax.broadcasted_iota(jnp.int32, (128, 128), 0) >= lax.broadcasted_iota(jnp.int32, (128, 128), 1))
        tri = tri.astype(jnp.bfloat16)

        def step(i, carry):
            rows = pl.ds(pl.multiple_of(i * 128, 128), 128)
            lf = _log_sigmoid(z_ref[rows, :] + b_ref[...])
            cb = _ones_dot_left(tri, lf) + carry
            c_ref[rows, :] = cb
            return cb[127:128, :]

        lax.fori_loop(0, nb, step, jnp.zeros((1, W), F32))

    return _pcall(body, name=name, out_shape=jax.ShapeDtypeStruct((S, W), F32),
                  compiler_params=pltpu.CompilerParams(vmem_limit_bytes=VMEM_LIMIT))(zf, bf)


def _fox_cumsum_bwd(dc, zf, bf, *, name):
    S, W = zf.shape
    nb = S // 128

    def body(dc_ref, z_ref, b_ref, dz_ref, db_ref):
        tri = (lax.broadcasted_iota(jnp.int32, (128, 128), 0) <= lax.broadcasted_iota(jnp.int32, (128, 128), 1))
        tri = tri.astype(jnp.bfloat16)

        def step(k, carry):
            tail, acc = carry
            i = nb - 1 - k
            rows = pl.ds(pl.multiple_of(i * 128, 128), 128)
            dlf = _ones_dot_left(tri, dc_ref[rows, :]) + tail
            dz = dlf * _sigmoid(-(z_ref[rows, :] + b_ref[...]))
            dz_ref[rows, :] = dz
            return dlf[0:1, :], acc + jnp.sum(dz, axis=0, keepdims=True)

        _, acc = lax.fori_loop(0, nb, step, (jnp.zeros((1, W), F32), jnp.zeros((1, W), F32)))
        db_ref[...] = acc

    return _pcall(body, name=name,
                  out_shape=[jax.ShapeDtypeStruct((S, W), F32), jax.ShapeDtypeStruct((1, W), F32)],
                  compiler_params=pltpu.CompilerParams(vmem_limit_bytes=VMEM_LIMIT))(dc, zf, bf)


def _proj_dil(h, w_qkv, *, name, tm=1024):
    S, D = h.shape
    tn = DIL_WIDTH

    def body(a_ref, b_ref, *rest):
        outs, acc = rest[:N_DIL_GROUPS], rest[N_DIL_GROUPS]
        prod = jnp.dot(a_ref[...].astype(_CD), b_ref[...].astype(_CD), preferred_element_type=F32)
        for k in range(tn // LANES):
            acc[k] = prod[:, k * LANES:(k + 1) * LANES]
        for g, (_, d) in enumerate(DIL_PAIRS):
            for half in range(DIL_OUT // LANES):
                k = g * (DIL_OUT // LANES) + half
                cols = slice(half * LANES, (half + 1) * LANES)
                for r in range(d):
                    rows = pl.ds(r, tm // d, stride=d) if d > 1 else slice(None)
                    outs[g][0, r, :, cols] = acc[k, rows, :].astype(outs[g].dtype)

    out_specs = [pl.BlockSpec((1, d, tm // d, DIL_OUT), lambda i, j: (j, 0, i, 0)) for _, d in DIL_PAIRS]
    out_shape = [jax.ShapeDtypeStruct((3, d, S // d, DIL_OUT), _CD) for _, d in DIL_PAIRS]
    outs = _pcall(body, name=name, grid=(S // tm, 3),
                  in_specs=[pl.BlockSpec((tm, D), lambda i, j: (i, 0)), pl.BlockSpec((D, tn), lambda i, j: (0, j))],
                  out_specs=out_specs, out_shape=out_shape, scratch_shapes=[pltpu.VMEM((tn // LANES, tm, LANES), F32)],
                  compiler_params=_params("parallel", "arbitrary"))(h, w_qkv)
    return [o.reshape(3, S, DIL_OUT) for o in outs]


def _dil_start(block, S, dilation):
    sub = S // dilation
    u0 = block * DIL_W
    return (u0 % sub) * dilation + u0 // sub


def _dil_slopes(group):
    h = np.arange(1, N_DIL_GROUPS * DIL_HEADS + 1, dtype=np.float32)
    s = (np.float32(2.0) ** (np.float32(-8.0) * h / np.float32(N_DIL_GROUPS * DIL_HEADS))).astype(np.float32)
    return [float(v) for v in s.reshape(N_DIL_GROUPS, DIL_HEADS)[group]]


def _dil_tiles(i, n, blocks_per_seq):
    qi = lax.broadcasted_iota(jnp.int32, (DIL_W, 2 * DIL_W), 0)
    kj = lax.broadcasted_iota(jnp.int32, (DIL_W, 2 * DIL_W), 1)
    rel = qi + DIL_W - kj
    first = ((4 * n + i) % blocks_per_seq) == 0
    valid = jnp.logical_and(jnp.logical_and(rel >= 0, rel <= DIL_W), jnp.logical_or(kj >= DIL_W, jnp.logical_not(first)))
    return valid, rel.astype(F32)


def _dil_window(cur_ref, prev_ref, i, cols):
    if i > 0:
        return cur_ref[(i - 1) * DIL_W:(i + 1) * DIL_W, cols]
    return jnp.concatenate([prev_ref[:, cols], cur_ref[:DIL_W, cols]], axis=0)


CHUNK = 4 * DIL_W


def _dil_rows(block, S, dilation):
    start = _dil_start(block, S, dilation)
    return pl.ds(start, DIL_W, stride=dilation) if dilation > 1 else pl.ds(start, DIL_W)


def SPLIT(S):
    return (DIL_OUT // LANES, S, LANES)


def _dil_fwd(qkv, group, *, name):
    S = qkv.shape[1]
    dilation = DIL_PAIRS[group][1]
    bps = (S // dilation) // DIL_W
    slopes = _dil_slopes(group)
    nt = (((1,), (1,)), ((), ()))

    def body(q_ref, k_ref, v_ref, kp_ref, vp_ref, on_ref, ln_ref, o_ref, l_ref):
        n = pl.program_id(0)
        for i in range(4):
            valid, rel = _dil_tiles(i, n, bps)
            rows = slice(i * DIL_W, (i + 1) * DIL_W)
            for h in range(DIL_HEADS):
                cols = slice(h * HEAD_DIM, (h + 1) * HEAD_DIM)
                qh = q_ref[rows, cols]
                k2, v2 = _dil_window(k_ref, kp_ref, i, cols), _dil_window(v_ref, vp_ref, i, cols)
                s = lax.dot_general(qh, k2, nt, preferred_element_type=F32) * ATTN_SCALE - (slopes[h] * dilation) * rel
                s = jnp.where(valid, s, NEG_INF)
                m = jnp.max(s, axis=-1, keepdims=True)
                p = jnp.exp(s - m)
                den = jnp.sum(p, axis=-1, keepdims=True)
                acc = jnp.dot(p.astype(_CD), v2, preferred_element_type=F32)
                o_ref[rows, cols] = acc / den
                l_ref[rows, cols] = jnp.broadcast_to(m + jnp.log(den), (DIL_W, HEAD_DIM))
        for i in range(4):
            rows = slice(i * DIL_W, (i + 1) * DIL_W)
            nat = _dil_rows(4 * n + i, S, dilation)
            for half in range(DIL_OUT // LANES):
                cols = slice(half * LANES, (half + 1) * LANES)
                on_ref[half, nat, :] = o_ref[rows, cols]
                ln_ref[half, nat, :] = l_ref[rows, cols]

    def cur(which):
        return pl.BlockSpec((None, CHUNK, DIL_OUT), lambda n: (which, n, 0))

    def prev(which):
        return pl.BlockSpec((None, DIL_W, DIL_OUT), lambda n: (which, jnp.maximum(4 * n - 1, 0), 0))

    whole = pl.BlockSpec(SPLIT(S), lambda n: (0, 0, 0))
    return _pcall(body, name=name, grid=(S // CHUNK,), in_specs=[cur(0), cur(1), cur(2), prev(1), prev(2)],
                  out_specs=[whole, whole],
                  out_shape=[jax.ShapeDtypeStruct(SPLIT(S), F32), jax.ShapeDtypeStruct(SPLIT(S), F32)],
                  scratch_shapes=[pltpu.VMEM((CHUNK, DIL_OUT), F32), pltpu.VMEM((CHUNK, DIL_OUT), F32)],
                  compiler_params=_params("arbitrary"))(qkv, qkv, qkv, qkv, qkv)


STAT_OFFSET = HEAD_DIM // 2


def _dil_bwd(qkv, stats, do, group, *, name):
    S = qkv.shape[1]
    dilation = DIL_PAIRS[group][1]
    bps = (S // dilation) // DIL_W
    slopes = _dil_slopes(group)
    nchunk = S // CHUNK
    nt = (((1,), (1,)), ((), ()))
    tn = (((0,), (0,)), ((), ()))

    def body(q_ref, k_ref, v_ref, kp_ref, vp_ref, ln_ref, don_ref, dqn_ref, dkn_ref, dvn_ref,
             dk_s, dv_s, l_ref, do_ref, dq_ref):
        step = pl.program_id(0)
        n = nchunk - 1 - step
        for i in range(4):
            rows = slice(i * DIL_W, (i + 1) * DIL_W)
            nat = _dil_rows(4 * n + i, S, dilation)
            for half in range(DIL_OUT // LANES):
                cols = slice(half * LANES, (half + 1) * LANES)
                l_ref[rows, cols] = ln_ref[half, nat, :]
                do_ref[rows, cols] = don_ref[half, nat, :]

        @pl.when(step == 0)
        def _():
            dk_s[:, CHUNK:] = jnp.zeros((DIL_OUT, DIL_W), F32)
            dv_s[:, CHUNK:] = jnp.zeros((DIL_OUT, DIL_W), F32)

        dk_s[:, :CHUNK] = jnp.zeros((DIL_OUT, CHUNK), F32)
        dv_s[:, :CHUNK] = jnp.zeros((DIL_OUT, CHUNK), F32)
        for i in range(4):
            valid, rel = _dil_tiles(i, n, bps)
            rows = slice(i * DIL_W, (i + 1) * DIL_W)
            window = slice(i * DIL_W, (i + 2) * DIL_W)
            for h in range(DIL_HEADS):
                cols = slice(h * HEAD_DIM, (h + 1) * HEAD_DIM)
                qh = q_ref[rows, cols]
                k2, v2 = _dil_window(k_ref, kp_ref, i, cols), _dil_window(v_ref, vp_ref, i, cols)
                lh = l_ref[rows, h * HEAD_DIM:h * HEAD_DIM + 1]
                shift = l_ref[rows, h * HEAD_DIM + STAT_OFFSET:h * HEAD_DIM + STAT_OFFSET + 1]
                s = lax.dot_general(qh, k2, nt, preferred_element_type=F32) * ATTN_SCALE - (slopes[h] * dilation) * rel
                p = jnp.exp(jnp.where(valid, s, NEG_INF) - lh)
                dob = do_ref[rows, cols].astype(_CD)
                ds = p * (lax.dot_general(dob, v2, nt, preferred_element_type=F32) + shift)
                dsb = (ds * ATTN_SCALE).astype(_CD)
                dq_ref[rows, cols] = jnp.dot(dsb, k2, preferred_element_type=F32)
                dk_s[cols, window] += lax.dot_general(qh, dsb, tn, preferred_element_type=F32)
                dv_s[cols, window] += lax.dot_general(dob, p.astype(_CD), tn, preferred_element_type=F32)
        for i in range(4):
            rows = slice(i * DIL_W, (i + 1) * DIL_W)
            done = slice((i + 1) * DIL_W, (i + 2) * DIL_W)
            nat = _dil_rows(4 * n + i, S, dilation)
            dkb, dvb = dk_s[:, done].T, dv_s[:, done].T
            for half in range(DIL_OUT // LANES):
                cols = slice(half * LANES, (half + 1) * LANES)
                dqn_ref[half, nat, :] = dq_ref[rows, cols]
                dkn_ref[half, nat, :] = dkb[:, cols]
                dvn_ref[half, nat, :] = dvb[:, cols]
        dk_s[:, CHUNK:] = dk_s[:, :DIL_W]
        dv_s[:, CHUNK:] = dv_s[:, :DIL_W]

    def cur(which):
        return pl.BlockSpec((None, CHUNK, DIL_OUT), lambda s: (which, nchunk - 1 - s, 0))

    def prev(which):
        return pl.BlockSpec((None, DIL_W, DIL_OUT), lambda s: (which, jnp.maximum(4 * (nchunk - 1 - s) - 1, 0), 0))

    whole = pl.BlockSpec(SPLIT(S), lambda s: (0, 0, 0))
    shp = jax.ShapeDtypeStruct(SPLIT(S), F32)
    tile = pltpu.VMEM((CHUNK, DIL_OUT), F32)
    return _pcall(body, name=name, grid=(nchunk,),
                  in_specs=[cur(0), cur(1), cur(2), prev(1), prev(2), whole, whole],
                  out_specs=[whole, whole, whole], out_shape=[shp, shp, shp],
                  scratch_shapes=[pltpu.VMEM((DIL_OUT, CHUNK + DIL_W), F32), pltpu.VMEM((DIL_OUT, CHUNK + DIL_W), F32),
                                  tile, tile, tile],
                  compiler_params=pltpu.CompilerParams(dimension_semantics=("arbitrary",),
                                                       vmem_limit_bytes=VMEM_LIMIT_RESIDENT))(
        qkv, qkv, qkv, qkv, qkv, stats, do)


def _dil_mix_fwd(os_, ls_, *, name, tm=512):
    nh, S, _ = os_[0].shape

    def body(o0, o1, o2, l0, l1, l2, out_ref):
        for half in range(nh):
            ls = [l0[half], l1[half], l2[half]]
            m = jnp.maximum(jnp.maximum(ls[0], ls[1]), ls[2])
            es = [jnp.exp(l - m) for l in ls]
            den = es[0] + es[1] + es[2]
            mixed = (es[0] * o0[half] + es[1] * o1[half] + es[2] * o2[half]) / den
            out_ref[:, half * LANES:(half + 1) * LANES] = mixed.astype(out_ref.dtype)

    halves = pl.BlockSpec((nh, tm, LANES), lambda i: (0, i, 0))
    row = pl.BlockSpec((tm, nh * LANES), lambda i: (i, 0))
    return _pcall(body, name=name, grid=(S // tm,), in_specs=[halves] * 6, out_specs=row,
                  out_shape=jax.ShapeDtypeStruct((S, nh * LANES), _CD), compiler_params=_params("parallel"))(*os_, *ls_)


def _dil_mix_bwd(doa, os_, ls_, *, name, tm=512, after=None):
    nh, S, _ = os_[0].shape

    def body(d_ref, o0, o1, o2, l0, l1, l2, do0, do1, do2, st0, st1, st2):
        first = lax.broadcasted_iota(jnp.int32, (tm, LANES), 1) % HEAD_DIM < STAT_OFFSET
        for half in range(nh):
            dv = d_ref[:, half * LANES:(half + 1) * LANES]
            ls = [l0[half], l1[half], l2[half]]
            m = jnp.maximum(jnp.maximum(ls[0], ls[1]), ls[2])
            es = [jnp.exp(l - m) for l in ls]
            den = es[0] + es[1] + es[2]
            al = [e / den for e in es]
            da = [_head_sum(dv * o[half]) for o in (o0, o1, o2)]
            mean = al[0] * da[0] + al[1] * da[1] + al[2] * da[2]
            for a, l, do_ref, st_ref in zip(al, ls, (do0, do1, do2), (st0, st1, st2)):
                do_ref[half] = a * dv
                st_ref[half] = jnp.where(first, l, -a * mean)

    halves = pl.BlockSpec((nh, tm, LANES), lambda i: (0, i, 0))
    row = pl.BlockSpec((tm, nh * LANES), lambda i: (i, 0))
    shp = jax.ShapeDtypeStruct((nh, S, LANES), F32)
    return _pcall(body, after, name=name, grid=(S // tm,), in_specs=[row] + [halves] * 6, out_specs=[halves] * 6,
                  out_shape=[shp] * 6, compiler_params=_params("parallel"))(doa, *os_, *ls_)


FOX_T = 512


PACK = 2 * HEAD_DIM
HEAD_PAIRS = N_FOX_HEADS // 2
FOX_HPS = 8
Q_BLOCK0 = 0
K_BLOCK0 = FOX_WIDTH // PACK
V_BLOCK0 = 2 * FOX_WIDTH // PACK


def _pieces(x):
    hi = x.astype(jnp.bfloat16).astype(F32)
    r = x - hi
    mid = r.astype(jnp.bfloat16).astype(F32)
    lo = (r - mid).astype(jnp.bfloat16).astype(F32)
    return [hi, mid, lo]


def _extras(first, second, rows):
    lane = lax.broadcasted_iota(jnp.int32, (rows, HEAD_DIM), 1)
    out = jnp.zeros((rows, HEAD_DIM), F32)
    for base, triple in ((0, first), (3, second)):
        if all(isinstance(v, float) for v in triple) and len(set(triple)) == 1:
            if triple[0] != 0.0:
                out = jnp.where(jnp.logical_and(lane >= base, lane < base + 3), triple[0], out)
        else:
            for idx, val in enumerate(triple):
                out = jnp.where(lane == base + idx, val, out)
    return out


def _head_column(c, h):
    lane = lax.broadcasted_iota(jnp.int32, c.shape, 1)
    return jnp.sum(jnp.where(lane == h, c, 0.0), axis=1, keepdims=True)


ONES3 = [1.0, 1.0, 1.0]
ZEROS3 = [0.0, 0.0, 0.0]


def _fox_pack_fwd(qkv, c, *, name, tm=1024):
    S = qkv.shape[0]

    def body(q_ref, k_ref, v_ref, c_ref, qo_ref, ko_ref, vo_ref):
        hp = pl.program_id(1)
        cv = c_ref[...]
        v_extras = jnp.where(lax.broadcasted_iota(jnp.int32, (tm, HEAD_DIM), 1) < 3, 1.0, 0.0).astype(vo_ref.dtype)
        for hh in range(2):
            ch = _pieces(_head_column(cv, 2 * hp + hh))
            src = slice(hh * HEAD_DIM, (hh + 1) * HEAD_DIM)
            lo = slice(hh * PACK, hh * PACK + HEAD_DIM)
            hi = slice(hh * PACK + HEAD_DIM, (hh + 1) * PACK)
            qo_ref[:, lo] = (q_ref[:, src].astype(F32) * ATTN_SCALE).astype(qo_ref.dtype)
            qo_ref[:, hi] = _extras(ch, ONES3, tm).astype(qo_ref.dtype)
            ko_ref[:, lo] = k_ref[:, src]
            ko_ref[:, hi] = _extras(ONES3, [-p for p in ch], tm).astype(ko_ref.dtype)
            vo_ref[:, lo] = v_ref[:, src]
            vo_ref[:, hi] = v_extras

    def src(block0):
        return pl.BlockSpec((tm, PACK), lambda i, hp: (i, block0 + hp))

    out = pl.BlockSpec((tm, 2 * PACK), lambda i, hp: (i, hp))
    shp = jax.ShapeDtypeStruct((S, N_FOX_HEADS * PACK), _CD)
    return _pcall(body, name=name, grid=(S // tm, HEAD_PAIRS),
                  in_specs=[src(Q_BLOCK0), src(K_BLOCK0), src(V_BLOCK0), pl.BlockSpec((tm, PACK), lambda i, hp: (i, 0))],
                  out_specs=[out, out, out], out_shape=[shp, shp, shp],
                  compiler_params=_params("parallel", "parallel"))(qkv, qkv, qkv, c)


def _fox_fwd(qp, kp, vp, *, name):
    S = qp.shape[0]
    nt = S // FOX_T
    nt_dims = (((1,), (1,)), ((), ()))
    tn_dims = (((0,), (0,)), ((), ()))

    def body(i_tab, j_tab, q_ref, k_ref, v_ref, o_ref, l_ref, m_s, acc_s):
        t = pl.program_id(1)
        i, j = i_tab[t], j_tab[t]

        @pl.when(j == 0)
        def _():
            m_s[...] = jnp.full((FOX_HPS, 1, FOX_T), NEG_INF, F32)
            acc_s[...] = jnp.zeros((FOX_HPS, PACK, FOX_T), F32)

        def tile(diagonal):
            for hh in range(FOX_HPS):
                cols = slice(hh * PACK, (hh + 1) * PACK)
                st = lax.dot_general(k_ref[:, cols], q_ref[:, cols], nt_dims, preferred_element_type=F32)
                if diagonal:
                    key = lax.broadcasted_iota(jnp.int32, (FOX_T, FOX_T), 0)
                    qry = lax.broadcasted_iota(jnp.int32, (FOX_T, FOX_T), 1)
                    st = jnp.where(key <= qry, st, NEG_INF)
                m_old = m_s[hh]
                m_new = jnp.maximum(m_old, jnp.max(st, axis=0, keepdims=True))
                pt = jnp.exp(st - m_new)
                acc_s[hh] = jnp.exp(m_old - m_new) * acc_s[hh] + lax.dot_general(
                    v_ref[:, cols], pt.astype(_CD), tn_dims, preferred_element_type=F32)
                m_s[hh] = m_new

        @pl.when(j < i)
        def _():
            tile(False)

        @pl.when(j == i)
        def _():
            tile(True)
            for hh in range(FOX_HPS):
                acc = acc_s[hh]
                den = acc[HEAD_DIM:HEAD_DIM + 1, :]
                cols = slice(hh * HEAD_DIM, (hh + 1) * HEAD_DIM)
                o_ref[:, cols] = (acc[:HEAD_DIM, :] / den).T
                l_ref[:, cols] = jnp.broadcast_to(m_s[hh] + jnp.log(den), (HEAD_DIM, FOX_T)).T

    pairs = [(i, j) for i in range(nt) for j in range(i + 1)]
    i_tab = jnp.asarray([p[0] for p in pairs], jnp.int32)
    j_tab = jnp.asarray([p[1] for p in pairs], jnp.int32)
    qs = pl.BlockSpec((FOX_T, FOX_HPS * PACK), lambda hp, t, it, jt: (it[t], hp))
    ks = pl.BlockSpec((FOX_T, FOX_HPS * PACK), lambda hp, t, it, jt: (jt[t], hp))
    os_ = pl.BlockSpec((FOX_T, FOX_HPS * HEAD_DIM), lambda hp, t, it, jt: (it[t], hp))
    shp = jax.ShapeDtypeStruct((S, FOX_WIDTH), F32)
    grid_spec = pltpu.PrefetchScalarGridSpec(
        num_scalar_prefetch=2, grid=(N_FOX_HEADS // FOX_HPS, len(pairs)), in_specs=[qs, ks, ks], out_specs=[os_, os_],
        scratch_shapes=[pltpu.VMEM((FOX_HPS, 1, FOX_T), F32), pltpu.VMEM((FOX_HPS, PACK, FOX_T), F32)])
    return _pcall(body, name=name, grid_spec=grid_spec, out_shape=[shp, shp],
                  compiler_params=_params("parallel", "arbitrary"))(i_tab, j_tab, qp, kp, vp)


def _fox_pack_bwd(qkv, c, o, lse, do, *, name, tm=1024, after=None):
    S = qkv.shape[0]

    def body(q_ref, c_ref, o_ref, l_ref, do_ref, qo_ref, do_out_ref):
        hp = pl.program_id(1)
        cv = c_ref[...]
        for hh in range(2):
            src = slice(hh * HEAD_DIM, (hh + 1) * HEAD_DIM)
            lo = slice(hh * PACK, hh * PACK + HEAD_DIM)
            hi = slice(hh * PACK + HEAD_DIM, (hh + 1) * PACK)
            shift = _head_column(cv, 2 * hp + hh) - l_ref[:, hh * HEAD_DIM:hh * HEAD_DIM + 1]
            dov = do_ref[:, src]
            dsum = jnp.sum(dov * o_ref[:, src], axis=-1, keepdims=True)
            qo_ref[:, lo] = (q_ref[:, src].astype(F32) * ATTN_SCALE).astype(qo_ref.dtype)
            qo_ref[:, hi] = _extras(_pieces(shift), ONES3, tm).astype(qo_ref.dtype)
            do_out_ref[:, lo] = dov.astype(do_out_ref.dtype)
            do_out_ref[:, hi] = _extras(_pieces(-dsum), ZEROS3, tm).astype(do_out_ref.dtype)

    pair = pl.BlockSpec((tm, PACK), lambda i, hp: (i, hp))
    out = pl.BlockSpec((tm, 2 * PACK), lambda i, hp: (i, hp))
    shp = jax.ShapeDtypeStruct((S, N_FOX_HEADS * PACK), _CD)
    return _pcall(body, after, name=name, grid=(S // tm, HEAD_PAIRS),
                  in_specs=[pl.BlockSpec((tm, PACK), lambda i, hp: (i, Q_BLOCK0 + hp)),
                            pl.BlockSpec((tm, PACK), lambda i, hp: (i, 0)), pair, pair, pair],
                  out_specs=[out, out], out_shape=[shp, shp],
                  compiler_params=_params("parallel", "parallel"))(qkv, c, o, lse, do)


def _fox_bwd(qp, kp, vp, dop, *, name):
    S = qp.shape[0]
    nt = S // FOX_T
    nt_dims = (((1,), (1,)), ((), ()))
    tn_dims = (((0,), (0,)), ((), ()))

    def body(i_tab, j_tab, q_ref, k_ref, v_ref, do_ref, dq_ref, dk_ref, dv_ref, dc_ref, dr_ref,
             dq_s, dk_s, dv_s, dc_s, dr_s):
        t = pl.program_id(1)
        i, j = i_tab[t], j_tab[t]

        @pl.when(t == 0)
        def _():
            dq_s[...] = jnp.zeros((S, FOX_HPS * PACK), F32)
            dr_s[...] = jnp.zeros((FOX_HPS, 1, S), F32)

        @pl.when(i == j)
        def _():
            dk_s[...] = jnp.zeros((FOX_T, FOX_HPS * PACK), F32)
            dv_s[...] = jnp.zeros((FOX_T, FOX_HPS * PACK), F32)
            dc_s[...] = jnp.zeros((FOX_HPS, FOX_T, 1), F32)

        def tile(diagonal):
            rows = pl.ds(pl.multiple_of(i * FOX_T, FOX_T), FOX_T)
            for hh in range(FOX_HPS):
                cols = slice(hh * PACK, (hh + 1) * PACK)
                qv, kv, vv, dov = q_ref[:, cols], k_ref[:, cols], v_ref[:, cols], do_ref[:, cols]
                pt = jnp.exp(lax.dot_general(kv, qv, nt_dims, preferred_element_type=F32))
                if diagonal:
                    key = lax.broadcasted_iota(jnp.int32, (FOX_T, FOX_T), 0)
                    qry = lax.broadcasted_iota(jnp.int32, (FOX_T, FOX_T), 1)
                    pt = jnp.where(key <= qry, pt, 0.0)
                dst = pt * lax.dot_general(vv, dov, nt_dims, preferred_element_type=F32)
                dsb = dst.astype(_CD)
                dc_s[hh] += jnp.sum(dst, axis=1, keepdims=True)
                dr_s[hh, :, rows] += jnp.sum(dst, axis=0, keepdims=True)
                dv_s[:, cols] += jnp.dot(pt.astype(_CD), dov, preferred_element_type=F32)
                dk_s[:, cols] += jnp.dot(dsb, qv, preferred_element_type=F32)
                dq_s[rows, cols] += lax.dot_general(dsb, kv, tn_dims, preferred_element_type=F32)

        @pl.when(i > j)
        def _():
            tile(False)

        @pl.when(i == j)
        def _():
            tile(True)

        @pl.when(i == nt - 1)
        def _():
            for hh in range(FOX_HPS):
                src = slice(hh * PACK, hh * PACK + HEAD_DIM)
                dst_cols = slice(hh * HEAD_DIM, (hh + 1) * HEAD_DIM)
                dk_ref[:, dst_cols] = dk_s[:, src].astype(dk_ref.dtype)
                dv_ref[:, dst_cols] = dv_s[:, src].astype(dv_ref.dtype)
                dc_ref[:, dst_cols] = jnp.broadcast_to(dc_s[hh], (FOX_T, HEAD_DIM))

        @pl.when(t == len(pairs) - 1)
        def _():
            for hh in range(FOX_HPS):
                dq_ref[:, hh * HEAD_DIM:(hh + 1) * HEAD_DIM] = (
                    dq_s[:, hh * PACK:hh * PACK + HEAD_DIM] * ATTN_SCALE).astype(dq_ref.dtype)
            dr_ref[...] = dr_s[...]

    pairs = [(i, j) for j in range(nt) for i in range(j, nt)]
    i_tab = jnp.asarray([p[0] for p in pairs], jnp.int32)
    j_tab = jnp.asarray([p[1] for p in pairs], jnp.int32)
    wide, narrow = FOX_HPS * PACK, FOX_HPS * HEAD_DIM
    qs = pl.BlockSpec((FOX_T, wide), lambda hp, t, it, jt: (it[t], hp))
    ks = pl.BlockSpec((FOX_T, wide), lambda hp, t, it, jt: (jt[t], hp))
    whole = pl.BlockSpec((S, narrow), lambda hp, t, it, jt: (0, hp))
    cs = pl.BlockSpec((FOX_T, narrow), lambda hp, t, it, jt: (jt[t], hp))
    rs = pl.BlockSpec((FOX_HPS, 1, S), lambda hp, t, it, jt: (hp, 0, 0))
    shp = jax.ShapeDtypeStruct((S, FOX_WIDTH), _CD)
    grid_spec = pltpu.PrefetchScalarGridSpec(
        num_scalar_prefetch=2, grid=(N_FOX_HEADS // FOX_HPS, len(pairs)), in_specs=[qs, ks, ks, qs],
        out_specs=[whole, cs, cs, cs, rs],
        scratch_shapes=[pltpu.VMEM((S, wide), F32), pltpu.VMEM((FOX_T, wide), F32),
                        pltpu.VMEM((FOX_T, wide), F32), pltpu.VMEM((FOX_HPS, FOX_T, 1), F32),
                        pltpu.VMEM((FOX_HPS, 1, S), F32)])
    return _pcall(body, name=name, grid_spec=grid_spec,
                  out_shape=[shp, shp, shp, jax.ShapeDtypeStruct((S, FOX_WIDTH), F32),
                             jax.ShapeDtypeStruct((N_FOX_HEADS, 1, S), F32)],
                  compiler_params=_params("parallel", "arbitrary"))(i_tab, j_tab, qp, kp, vp, dop)


def _layer_step(x, tgt, w, p, late_weights=None, grad_sink=None, after=None, first_weights=None):
    S = x.shape[0]
    after_norm, after_proj = after if after is not None else (None, None)
    h = _rms_fwd(x, p["norm_mix_g"], name="rms_mix", after=after_norm)
    if first_weights is not None:
        w = {**w, **first_weights(h)}
    qkv = _mm(h, w["qkv"][:, 3 * DIL_WIDTH:], name="proj_fox", out_dtype=_CD, tn=768, tm=2048, after=after_proj)
    dil_qkv = _proj_dil(h, w["qkv"], name="proj_dil")
    zf = _mm(h, w["f"], name="proj_f")
    gl = _mm(h, w["g"], name="proj_gate", tn=1024, out_dtype=_CD)

    dil_o, dil_l = [], []
    for g in range(N_DIL_GROUPS):
        og, lg = _dil_fwd(dil_qkv[g], g, name=f"dil_fwd{g}")
        dil_o.append(og), dil_l.append(lg)
    o_a = _dil_mix_fwd(dil_o, dil_l, name="dil_mix")

    c = _fox_cumsum(zf, p["b_fgt"], name="fox_cumsum")
    fqp, fkp, fvp = _fox_pack_fwd(qkv, c, name="fox_pack")
    o_b, flse = _fox_fwd(fqp, fkp, fvp, name="fox_fwd")

    if late_weights is not None:
        w = {**w, **late_weights(o_b)}
    y_a = _mm(o_a, w["dil_out"], name="y_a", tn=1024, out_dtype=_CD)
    y_b = _mm(o_b, w["fox_out"], name="y_b", tn=1024, out_dtype=_CD)
    merged, x1, h2 = _gated_mix_out(gl, p["b_gate"], y_a, y_b, w["out"], x, p["norm_ffn_g"], name="mix_out")
    gate, up, act = _ffn_in_act(h2, w["ffn_in"], name="ffn_in")
    loss, dx2, dg_final = _ffn_down_loss(act, w["ffn_down"], x1, p["norm_final_g"], tgt, name="ffn_down_loss")

    gw_ffn_down = _mm(act, dx2, name="gw_ffn_down", ta=True, out_dtype=_CD, tm=1408)
    dgu = _d_swiglu(dx2, w["ffn_down"], gate, up, name="d_swiglu")
    gw_ffn_in = _mm(h2, dgu, name="gw_ffn_in", ta=True, out_dtype=_CD, tn=1408, out_blocks=1408, b_halves=True)
    sink = grad_sink if grad_sink is not None else (lambda group, grads: None)
    tok = sink("ffn", dict(ffn_in=gw_ffn_in, ffn_down=gw_ffn_down))
    dx1, dg_ffn = _d_h2(dgu, w["ffn_in"], x1, p["norm_ffn_g"], dx2, name="d_h2", after=tok)

    gw_out = _mm(merged, dx1, name="gw_out", ta=True, out_dtype=_CD)
    dy_a, dy_b, dgl, db_gate = _gate_bwd(dx1, w["out"], gl, p["b_gate"], y_a, y_b, name="gate_bwd")
    do_a = _mm(dy_a, w["dil_out"], name="d_o_a", tb=True)
    gw_dil_out = _mm(o_a, dy_a, name="gw_dil_out", ta=True, out_dtype=_CD, tn=1024)
    do_b = _mm(dy_b, w["fox_out"], name="d_o_b", tb=True)
    gw_fox_out = _mm(o_b, dy_b, name="gw_fox_out", ta=True, out_dtype=_CD, tn=1024)
    tok = sink("mix", dict(dil_out=gw_dil_out, fox_out=gw_fox_out, out=gw_out))

    bqp, bdop = _fox_pack_bwd(qkv, c, o_b, flse, do_b, name="fox_pack_bwd", after=tok)
    dqp, dkp, dvp, dck, dcq = _fox_bwd(bqp, fkp, fvp, bdop, name="fox_bwd")
    dc = dcq[:, 0, :].T - dck.reshape(S, N_FOX_HEADS, HEAD_DIM)[:, :, 0]
    dc = jnp.pad(dc, ((0, 0), (0, F_PAD - N_FOX_HEADS)))
    dzf, db_fgt = _fox_cumsum_bwd(dc, zf, p["b_fgt"], name="fox_cumsum_bwd")

    douts = _dil_mix_bwd(do_a, dil_o, dil_l, name="dil_mix_bwd", after=tok)
    dqs, dks, dvs = [], [], []
    for g in range(N_DIL_GROUPS):
        dq, dk, dv = _dil_bwd(dil_qkv[g], douts[3 + g], douts[g], g, name=f"dil_bwd{g}")
        for parts, t in ((dqs, dq), (dks, dk), (dvs, dv)):
            parts.extend([t[0].astype(_CD), t[1].astype(_CD)])
    dqkv = jnp.concatenate(dqs + dks + dvs + [dqp, dkp, dvp], axis=1)

    gw_qkv = _mm(h, dqkv, name="gw_qkv", ta=True, out_dtype=_CD, tn=768, tk=S)
    gw_g = _mm(h, dgl, name="gw_gate", ta=True, out_dtype=_CD, tk=S)
    gw_f = _mm(h, dzf, name="gw_f", ta=True, out_dtype=_CD)
    tok = sink("in", dict(qkv=gw_qkv, f=gw_f, g=gw_g))
    dx, dg_mix = _mm(dqkv, w["qkv"], name="d_h", tb=True, tk=QKV_COLS, tm=256, more=((dgl, w["g"]), (dzf, w["f"])),
                     rms_bwd=(x, p["norm_mix_g"], dx1), after=tok)

    gw = dict(qkv=gw_qkv, f=gw_f, g=gw_g, dil_out=gw_dil_out, fox_out=gw_fox_out, out=gw_out, ffn_in=gw_ffn_in,
              ffn_down=gw_ffn_down)
    small = dict(norm_mix_g=dg_mix, b_fgt=db_fgt, b_gate=db_gate, norm_ffn_g=dg_ffn, norm_final_g=dg_final)
    return loss, dx, gw, small


def _position():
    return lax.axis_index("x"), lax.axis_index("y"), lax.axis_index("c")


def _other_chips(x, y):
    return [(1 - x, y), (x, 1 - y), (1 - x, 1 - y)]


ROW_TILE = 16


def _row_chunks(rows, want=4):
    n = want
    while n > 1 and rows % (n * ROW_TILE):
        n //= 2
    return n


SEM_SPEC = pl.BlockSpec(memory_space=pltpu.SEMAPHORE)
ANY_SPEC = pl.BlockSpec(memory_space=pl.ANY)
DATAFLOW = pltpu.SideEffectType.DATAFLOW_SIDE_EFFECTING


def _in_hbm(a):
    return pltpu.with_memory_space_constraint(a, pltpu.HBM)


def _split_copy_start(srcs, land_shapes, copies, after, *, name):
    n, m = len(srcs), len(land_shapes)

    def body(*refs):
        src_refs, land_refs = refs[:n], refs[n:n + m]
        send_sems, recv_sems = refs[n + m + 1], refs[n + m + 2]
        token = refs[-1]
        x, y, c = _position()
        for k, (src, dst, peer) in enumerate(copies(x, y, c, src_refs, land_refs)):
            pltpu.make_async_remote_copy(src_ref=src, dst_ref=dst, send_sem=send_sems.at[k], recv_sem=recv_sems.at[k],
                                         device_id=peer, device_id_type=MESH).start()
        token[...] = jnp.zeros_like(token)

    lands = [lax.empty(s.shape, s.dtype) for s in land_shapes]
    count = len(copies(0, 0, 0, srcs, lands))
    out = _pcall(
        body, name=name,
        out_shape=(pltpu.SemaphoreType.DMA((count,)), pltpu.SemaphoreType.DMA((count,)),
                   *[pltpu.HBM(s.shape, s.dtype) for s in srcs], *[pltpu.HBM(s.shape, s.dtype) for s in land_shapes],
                   jax.ShapeDtypeStruct((8, 128), F32)),
        in_specs=[HBM_SPEC] * (n + m) + [ANY_SPEC],
        out_specs=(SEM_SPEC, SEM_SPEC, *[HBM_SPEC] * (n + m), pl.BlockSpec(memory_space=pltpu.VMEM)),
        input_output_aliases={k: 2 + k for k in range(n + m)},
        compiler_params=pltpu.CompilerParams(has_side_effects=DATAFLOW),
    )(*[_in_hbm(s) for s in srcs], *[_in_hbm(l) for l in lands], after)
    return out[0], out[1], list(out[2:2 + n]), list(out[2 + n:2 + n + m]), out[-1]


def _split_copy_wait(send_sems, recv_sems, srcs, lands, copies, after, *, name):
    n, m = len(srcs), len(lands)

    def body(*refs):
        src_refs, land_refs = refs[:n], refs[n:n + m]
        send, recv = refs[n + m], refs[n + m + 1]
        x, y, c = _position()
        for k, (src, dst, peer) in enumerate(copies(x, y, c, src_refs, land_refs)):
            cp = pltpu.make_async_remote_copy(src_ref=src, dst_ref=dst, send_sem=send.at[k], recv_sem=recv.at[k],
                                              device_id=peer, device_id_type=MESH)
            cp.wait_send()
            cp.wait_recv()

    afters = list(after) if isinstance(after, (list, tuple)) else [after]
    out = _pcall(
        body, name=name,
        out_shape=tuple(pltpu.HBM(s.shape, s.dtype) for s in list(srcs) + list(lands)),
        in_specs=[HBM_SPEC] * (n + m) + [SEM_SPEC, SEM_SPEC] + [ANY_SPEC] * len(afters),
        out_specs=tuple([HBM_SPEC] * (n + m)),
        input_output_aliases={k: k for k in range(n + m)},
        compiler_params=pltpu.CompilerParams(has_side_effects=DATAFLOW),
    )(*srcs, *lands, send_sems, recv_sems, *afters)
    return list(out[:n]), list(out[n:])


def _gather_copies(x, y, c, shard_refs, land_refs):
    out = []
    for s, l in zip(shard_refs, land_refs):
        half = s.shape[0] // 2
        nq = _row_chunks(half)
        for cx, cy in _other_chips(x, y):
            for q in range(nq):
                rows = pl.ds(c * half + q * (half // nq), half // nq)
                out.append((s.at[rows, :], l.at[2 * x + y, rows, :], (cx, cy, c)))
    return out


def _gather_whole_copies(x, y, c, shard_refs, land_refs):
    out = []
    for s, l in zip(shard_refs, land_refs):
        nq = _row_chunks(s.shape[0])
        for cx, cy in _other_chips(x, y):
            for q in range(nq):
                rows = pl.ds(q * (s.shape[0] // nq), s.shape[0] // nq)
                out.append((s.at[rows, :], l.at[2 * x + y, rows, :], (cx, cy, c)))
    return out


def _scatter_all_copies(x, y, c, block_refs, land_refs):
    out = []
    for g, l in zip(block_refs, land_refs):
        half = g.shape[1] // 2
        nq = _row_chunks(half)
        size = half // nq
        for q in range(nq):
            rows = pl.ds((1 - c) * half + q * size, size)
            out.append((g.at[2 * x + y, rows, :], l.at[0, pl.ds(q * size, size), :], (x, y, 1 - c)))
        for r, (cx, cy) in enumerate(_other_chips(x, y)):
            for j in range(2):
                h = c if j == 0 else 1 - c
                for q in range(nq):
                    rows = pl.ds(h * half + q * size, size)
                    out.append((g.at[2 * cx + cy, rows, :], l.at[1 + 2 * r + j, pl.ds(q * size, size), :], (cx, cy, h)))
    return out


def _forward_halves(lands, *, name):
    n = len(lands)

    def body(*refs):
        ins = refs[:n]
        send_sems, recv_sems = refs[2 * n:]
        x, y, c = _position()
        copies = []
        for w in range(n):
            half = ins[w].shape[1] // 2
            for r, (cx, cy) in enumerate(_other_chips(x, y)):
                blk = ins[w].at[2 * cx + cy, pl.ds(c * half, half), :]
                cp = pltpu.make_async_remote_copy(src_ref=blk, dst_ref=blk, send_sem=send_sems.at[w, r],
                                                  recv_sem=recv_sems.at[w, r], device_id=(x, y, 1 - c),
                                                  device_id_type=MESH)
                cp.start()
                copies.append(cp)
        for w in range(n):
            half = ins[w].shape[1] // 2
            for r, (cx, cy) in enumerate(_other_chips(x, y)):
                blk = ins[w].at[2 * cx + cy, pl.ds((1 - c) * half, half), :]
                pltpu.make_async_remote_copy(src_ref=blk, dst_ref=blk, send_sem=send_sems.at[w, r],
                                             recv_sem=recv_sems.at[w, r], device_id=(x, y, 1 - c),
                                             device_id_type=MESH).wait_recv()
        for cp in copies:
            cp.wait_send()

    return _pcall(
        body, name=name, in_specs=[HBM_SPEC] * n, out_specs=[HBM_SPEC] * n,
        out_shape=[jax.ShapeDtypeStruct(l.shape, l.dtype) for l in lands],
        input_output_aliases={k: k for k in range(n)},
        scratch_shapes=[pltpu.SemaphoreType.DMA((n, 3)), pltpu.SemaphoreType.DMA((n, 3))],
    )(*lands)


def _share_halves(halves, *, name):
    n = len(halves)

    def body(*refs):
        ins, outs = refs[:n], refs[n:2 * n]
        send_sems, recv_sems = refs[2 * n:]
        x, y, c = _position()
        copies = []
        for w in range(n):
            cp = pltpu.make_async_remote_copy(src_ref=ins[w], dst_ref=outs[w], send_sem=send_sems.at[w],
                                              recv_sem=recv_sems.at[w], device_id=(x, y, 1 - c), device_id_type=MESH)
            cp.start()
            copies.append(cp)
        for cp in copies:
            cp.wait()

    return _pcall(
        body, name=name, in_specs=[HBM_SPEC] * n, out_specs=[HBM_SPEC] * n,
        out_shape=[jax.ShapeDtypeStruct(h.shape, h.dtype) for h in halves],
        scratch_shapes=[pltpu.SemaphoreType.DMA((n,)), pltpu.SemaphoreType.DMA((n,))],
    )(*halves)


def _sum_small(part, after=None):
    rows, width = part.shape

    def body(x_ref, out_ref, all_ref, send_sems, recv_sems):
        x, y, c = _position()
        me, sibling = (x, y, c), (x, y, 1 - c)
        chips = _other_chips(x, y)

        def block(px, py, pc):
            return all_ref.at[pl.ds((4 * px + 2 * py + pc) * rows, rows), :]

        def copy(k, blk, to, src=None):
            return pltpu.make_async_remote_copy(
                src_ref=block(*blk) if src is None else src, dst_ref=block(*blk), send_sem=send_sems.at[k],
                recv_sem=recv_sems.at[k], device_id=to, device_id_type=MESH)

        all_ref[pl.ds((4 * x + 2 * y + c) * rows, rows), :] = x_ref[...]
        first = [copy(0, me, sibling, src=x_ref)]
        first += [copy(1 + j, me, (*chip, c), src=x_ref) for j, chip in enumerate(chips)]
        for cp in first:
            cp.start()
        passed = [copy(4 + j, (*chip, c), sibling) for j, chip in enumerate(chips)]
        for j, chip in enumerate(chips):
            copy(1 + j, (*chip, c), me).wait_recv()
            passed[j].start()
        copy(0, sibling, me).wait_recv()
        for j, chip in enumerate(chips):
            copy(4 + j, (*chip, 1 - c), me).wait_recv()
        for cp in first + passed:
            cp.wait_send()
        total = all_ref[0:rows, :]
        for d in range(1, 8):
            total = total + all_ref[d * rows:(d + 1) * rows, :]
        out_ref[...] = total

    vm = pl.BlockSpec(memory_space=pltpu.VMEM)
    return _pcall(
        body, after, name="sum_small", in_specs=[vm], out_specs=vm, out_shape=jax.ShapeDtypeStruct((rows, width), F32),
        scratch_shapes=[pltpu.VMEM((8 * rows, width), F32), pltpu.SemaphoreType.DMA((7,)), pltpu.SemaphoreType.DMA((7,))],
    )(part)


def _row_tile(R, C, itemsize=4, budget=1 << 20):
    fits = [t for t in range(ROW_TILE, R + 1, ROW_TILE) if R % t == 0 and t * C * itemsize <= budget]
    return max(fits) if fits else R


def _add_all(g, recv, where, *, name):
    _, R, C = g.shape
    half = R // 2
    t = _row_tile(half, C)
    nb = half // t

    def body(w_ref, g_ref, r_ref, o_ref):
        total = g_ref[0].astype(F32)
        for k in range(7):
            total = total + r_ref[k].astype(F32)
        o_ref[...] = total

    grid_spec = pltpu.PrefetchScalarGridSpec(
        num_scalar_prefetch=1, grid=(nb,),
        in_specs=[pl.BlockSpec((1, t, C), lambda i, wr: (wr[0], wr[1] * nb + i, 0)),
                  pl.BlockSpec((7, t, C), lambda i, wr: (0, i, 0))],
        out_specs=pl.BlockSpec((t, C), lambda i, wr: (i, 0)))
    return _pcall(body, name=name, grid_spec=grid_spec, out_shape=jax.ShapeDtypeStruct((half, C), F32),
                  compiler_params=_params("parallel"))(where, g, recv)


def _adamw(w, g, m, v, *, name):
    R, C = w.shape
    t = _row_tile(R, C)
    c1 = 1.0 - ADAM_B1 ** ADAM_STEP
    c2 = 1.0 - ADAM_B2 ** ADAM_STEP

    def body(w_ref, g_ref, m_ref, v_ref, d_ref, nm_ref, nv_ref):
        gv = g_ref[...]
        mn = ADAM_B1 * m_ref[...] + (1.0 - ADAM_B1) * gv
        vn = ADAM_B2 * v_ref[...] + (1.0 - ADAM_B2) * (gv * gv)
        d_ref[...] = -ADAM_LR * ((mn / c1) / (jnp.sqrt(vn / c2) + ADAM_EPS) + ADAM_WD * w_ref[...])
        nm_ref[...] = mn
        nv_ref[...] = vn

    blk = pl.BlockSpec((t, C), lambda i: (i, 0))
    shp = jax.ShapeDtypeStruct((R, C), F32)
    return _pcall(body, name=name, grid=(R // t,), in_specs=[blk] * 4, out_specs=[blk] * 3, out_shape=[shp] * 3,
                  compiler_params=_params("parallel"))(w, g, m, v)


def _adamw_halves(w, mine, theirs, m, v, core, *, name, after=None):
    R, C = w.shape
    half = R // 2
    t = _row_tile(half, C)
    nbh = half // t
    c1 = 1.0 - ADAM_B1 ** ADAM_STEP
    c2 = 1.0 - ADAM_B2 ** ADAM_STEP

    def body(core_ref, w_ref, a_ref, b_ref, m_ref, v_ref, *rest):
        g_ref, d_ref, nm_ref, nv_ref = rest[-4:]
        gv = jnp.where(pl.program_id(0) // nbh == core_ref[0], a_ref[...], b_ref[...])
        mn = ADAM_B1 * m_ref[...] + (1.0 - ADAM_B1) * gv
        vn = ADAM_B2 * v_ref[...] + (1.0 - ADAM_B2) * (gv * gv)
        g_ref[...] = gv
        d_ref[...] = -ADAM_LR * ((mn / c1) / (jnp.sqrt(vn / c2) + ADAM_EPS) + ADAM_WD * w_ref[...])
        nm_ref[...] = mn
        nv_ref[...] = vn

    blk = pl.BlockSpec((t, C), lambda i, cr: (i, 0))
    hblk = pl.BlockSpec((t, C), lambda i, cr: (i % nbh, 0))
    shp = jax.ShapeDtypeStruct((1, R, C), F32)
    oblk = pl.BlockSpec((None, t, C), lambda i, cr: (0, i, 0))
    tied = [] if after is None else [after]
    grid_spec = pltpu.PrefetchScalarGridSpec(num_scalar_prefetch=1, grid=(2 * nbh,),
                                             in_specs=[blk, hblk, hblk, blk, blk] + [ANY_SPEC] * len(tied),
                                             out_specs=[oblk] * 4)
    return _pcall(body, name=name, grid_spec=grid_spec, out_shape=[shp] * 4,
                  compiler_params=_params("parallel"))(core, w, mine, theirs, m, v, *tied)


BIG = ("w_in", "w_dil_out", "w_fox_out", "w_out", "w_ffn_in", "w_ffn_down")
SMALL = ("norm_mix_g", "b_fgt", "b_gate", "norm_ffn_g", "norm_final_g")
ORDER = ("norm_mix_g", "w_in", "b_fgt", "b_gate", "w_dil_out", "w_fox_out", "w_out", "norm_ffn_g", "w_ffn_in",
         "w_ffn_down", "norm_final_g")
SMALL_ROWS = {"norm_mix_g": (0, 1), "b_gate": (1, 3), "norm_ffn_g": (3, 4), "norm_final_g": (4, 5), "b_fgt": (5, 6)}


def _columns_to_blocks(full, ncol):
    K = full.shape[0]
    return full.reshape(K, 4, ncol).transpose(1, 0, 2)


def _pieces_to_blocks(pieces, ncol):
    spans, start = [], 0
    for piece in pieces:
        spans.append((piece, start, start + piece.shape[1]))
        start += piece.shape[1]
    assert start == 4 * ncol
    blocks = []
    for k in range(4):
        lo, hi = k * ncol, (k + 1) * ncol
        parts = [p[:, max(lo, a) - a:min(hi, b) - a] for p, a, b in spans if a < hi and b > lo]
        blocks.append(parts[0] if len(parts) == 1 else jnp.concatenate(parts, axis=1))
    return jnp.stack(blocks)


def _blocks_to_pieces(blocks, widths):
    n, K, ncol = blocks.shape
    assert sum(widths) == n * ncol
    pieces, lo = [], 0
    for width in widths:
        hi = lo + width
        parts = [blocks[k][:, max(lo, k * ncol) - k * ncol:min(hi, (k + 1) * ncol) - k * ncol]
                 for k in range(n) if k * ncol < hi and (k + 1) * ncol > lo]
        pieces.append(parts[0] if len(parts) == 1 else jnp.concatenate(parts, axis=1))
        lo = hi
    return pieces


def _blocks_to_columns(blocks):
    n, K, ncol = blocks.shape
    return blocks.transpose(1, 0, 2).reshape(K, n * ncol)


def kernel(x, norm_mix_g, w_in, b_fgt, b_gate, w_dil_out, w_fox_out, w_out, norm_ffn_g, w_ffn_in, w_ffn_down, norm_final_g, loss_target, m_norm_mix_g, m_w_in, m_b_fgt, m_b_gate, m_w_dil_out, m_w_fox_out, m_w_out, m_norm_ffn_g, m_w_ffn_in, m_w_ffn_down, m_norm_final_g, v_norm_mix_g, v_w_in, v_b_fgt, v_b_gate, v_w_dil_out, v_w_fox_out, v_w_out, v_norm_ffn_g, v_w_ffn_in, v_w_ffn_down, v_norm_final_g):
    weights = dict(norm_mix_g=norm_mix_g, w_in=w_in, b_fgt=b_fgt, b_gate=b_gate, w_dil_out=w_dil_out,
                   w_fox_out=w_fox_out, w_out=w_out, norm_ffn_g=norm_ffn_g, w_ffn_in=w_ffn_in, w_ffn_down=w_ffn_down,
                   norm_final_g=norm_final_g)
    m_in = dict(norm_mix_g=m_norm_mix_g, w_in=m_w_in, b_fgt=m_b_fgt, b_gate=m_b_gate, w_dil_out=m_w_dil_out,
                w_fox_out=m_w_fox_out, w_out=m_w_out, norm_ffn_g=m_norm_ffn_g, w_ffn_in=m_w_ffn_in,
                w_ffn_down=m_w_ffn_down, norm_final_g=m_norm_final_g)
    v_in = dict(norm_mix_g=v_norm_mix_g, w_in=v_w_in, b_fgt=v_b_fgt, b_gate=v_b_gate, w_dil_out=v_w_dil_out,
                w_fox_out=v_w_fox_out, w_out=v_w_out, norm_ffn_g=v_norm_ffn_g, w_ffn_in=v_w_ffn_in,
                w_ffn_down=v_w_ffn_down, norm_final_g=v_norm_final_g)
    c = lax.axis_index("c")
    chip = 2 * lax.axis_index("x") + lax.axis_index("y")

    shards = {n: weights[n][0].astype(_CD) for n in BIG}
    in_shape = jax.ShapeDtypeStruct((4,) + shards["w_in"].shape, _CD)
    send_i, recv_i, in_src, in_land, token_in = _split_copy_start(
        [shards["w_in"]], [in_shape], _gather_copies, norm_mix_g, name="gather_in_start")
    late = BIG[1:]
    send_g, recv_g, late_src, late_land, token = _split_copy_start(
        [shards[n] for n in late], [jax.ShapeDtypeStruct((4,) + shards[n].shape, _CD) for n in late],
        _gather_whole_copies, token_in, name="gather_late_start")
    adam_in = [t[0] + token_in[0, 0] for t in (w_in, m_w_in, v_w_in)]
    p = dict(norm_mix_g=norm_mix_g, b_fgt=jnp.pad(b_fgt, ((0, 0), (0, F_PAD - N_FOX_HEADS))), b_gate=b_gate,
             norm_ffn_g=norm_ffn_g, norm_final_g=norm_final_g.reshape(1, D_MODEL))

    def first_weights(after):
        own, lands = _split_copy_wait(send_i, recv_i, in_src, in_land, _gather_copies, [after] + adam_in,
                                      name="gather_in_wait")
        (g_in,) = _forward_halves(lands, name="gather_in_forward")
        blocks = lax.dynamic_update_index_in_dim(g_in, own[0], chip, 0)
        qkv, f, g = _blocks_to_pieces(blocks, (QKV_COLS, N_FOX_HEADS, 2 * D_MODEL))
        return dict(qkv=qkv, f=jnp.pad(f, ((0, 0), (0, F_PAD - N_FOX_HEADS))), g=g)

    def late_weights(after):
        own, lands = _split_copy_wait(send_g, recv_g, late_src, late_land, _gather_whole_copies, after,
                                      name="gather_late_wait")
        g_dil, g_fox, g_out, g_ffn_in, g_ffn_down = [
            lax.dynamic_update_index_in_dim(l, s, chip, 0) for l, s in zip(lands, own)]
        return dict(dil_out=_blocks_to_columns(g_dil), fox_out=_blocks_to_columns(g_fox),
                    out=g_out.reshape(D_MODEL, D_MODEL), ffn_in=g_ffn_in,
                    ffn_down=g_ffn_down.reshape(D_FF, D_MODEL))

    def to_blocks(n, full):
        shape = weights[n].shape
        if full.ndim == 3:
            return full
        if n in ("w_out", "w_ffn_down"):
            return full.reshape(4, shape[1], shape[2])
        return _columns_to_blocks(full, shape[2])

    in_flight = {}

    def grad_sink(group, gw):
        if group == "in":
            named = {"w_in": _pieces_to_blocks([gw["qkv"], gw["f"][:, :N_FOX_HEADS], gw["g"]], weights["w_in"].shape[2])}
        else:
            named = {"w_" + k: v for k, v in gw.items()}
        srcs = [to_blocks(n, named[n]) for n in named]
        lands = [jax.ShapeDtypeStruct((7, s.shape[1] // 2, s.shape[2]), s.dtype) for s in srcs]
        started = _split_copy_start(srcs, lands, _scatter_all_copies, next(iter(gw.values())),
                                    name=f"scatter_{group}_start")
        in_flight[group] = (list(named), started)
        return started[-1]

    loss_part, grad_x, gw, small = _layer_step(x[0], loss_target[0], {}, p, late_weights, grad_sink,
                                               (token_in, token), first_weights)

    where = jnp.stack([chip, c]).astype(jnp.int32)

    def summed_halves(groups, after, name):
        halves = {}
        for group in groups:
            names, (send_s, recv_s, srcs, lands, _) = in_flight[group]
            srcs, recv = _split_copy_wait(send_s, recv_s, srcs, lands, _scatter_all_copies, after,
                                          name=f"scatter_{group}_wait")
            halves.update({n: _add_all(s, r, where, name=f"add_all_{n}") for n, s, r in zip(names, srcs, recv)})
        return {n: (h, o) for (n, h), o in zip(halves.items(), _share_halves(list(halves.values()), name=name))}

    grad_halves = summed_halves(("ffn", "mix"), grad_x, "share_halves")

    out_g, out_d, out_m, out_v = {}, {}, {}, {}
    core = jnp.reshape(c, (1,)).astype(jnp.int32)

    def adamw_big(n, after=None):
        shape = weights[n].shape
        wmv = adam_in if n == "w_in" else [t[0] for t in (weights[n], m_in[n], v_in[n])]
        mine, theirs = grad_halves[n]
        outs = _adamw_halves(wmv[0], mine, theirs, wmv[1], wmv[2], core, name=f"adamw_{n}", after=after)
        out_g[n], out_d[n], out_m[n], out_v[n] = [t.reshape(shape) for t in outs]

    early = ("w_dil_out", "w_fox_out", "w_out", "w_ffn_down", "w_ffn_in")
    for n in early:
        adamw_big(n)
    grad_halves.update(summed_halves(("in",), [grad_x] + [out_d[n] for n in early], "share_halves_in"))
    adamw_big("w_in")

    packed = jnp.concatenate([
        small["norm_mix_g"], small["b_gate"].reshape(2, D_MODEL), small["norm_ffn_g"], small["norm_final_g"],
        jnp.pad(small["b_fgt"], ((0, 0), (0, D_MODEL - F_PAD))), jnp.pad(loss_part, ((0, 0), (0, D_MODEL - 1))),
        jnp.zeros((1, D_MODEL), F32)], axis=0)
    summed = _sum_small(packed, after=out_d["w_in"])
    loss = summed[6, 0]

    for n in SMALL:
        lo, hi = SMALL_ROWS[n]
        shape = weights[n].shape
        g2 = summed[lo:hi].reshape(1, -1)[:, :weights[n].size]
        d2, m2, v2 = _adamw(weights[n].reshape(g2.shape), g2, m_in[n].reshape(g2.shape), v_in[n].reshape(g2.shape),
                            name=f"adamw_{n}")
        out_g[n], out_d[n], out_m[n], out_v[n] = [t.reshape(shape) for t in (g2, d2, m2, v2)]
    return (loss, grad_x[None], *[out_g[n] for n in ORDER], *[out_d[n] for n in ORDER],
            *[out_m[n] for n in ORDER], *[out_v[n] for n in ORDER])
```

```python
import numpy as np
import jax
import jax.numpy as jnp
from jax import lax
from jax.experimental import pallas as pl
from jax.experimental.pallas import tpu as pltpu

F32 = jnp.float32
_CD = jnp.bfloat16

D_MODEL = 1024
HEAD_DIM = 64
DIL_PAIRS = ((128, 1), (512, 4), (2048, 16))
N_DIL_GROUPS = 3
DIL_HEADS = 4
DIL_W = 128
DIL_OUT = DIL_HEADS * HEAD_DIM
DIL_WIDTH = N_DIL_GROUPS * DIL_OUT
N_FOX_HEADS = 8
FOX_WIDTH = N_FOX_HEADS * HEAD_DIM
D_FF = 2816
QKV_COLS = 3 * DIL_WIDTH + 3 * FOX_WIDTH
F_PAD = 128
RMS_EPS = 1e-6
NEG_INF = -1e30
ATTN_SCALE = HEAD_DIM ** -0.5
ADAM_LR, ADAM_B1, ADAM_B2, ADAM_EPS, ADAM_WD, ADAM_STEP = 0.001, 0.9, 0.999, 1e-08, 0.01, 10

VMEM_LIMIT = 48 * 1024 * 1024
VMEM_LIMIT_RESIDENT = 56 * 1024 * 1024
LANES = 128
MESH = pl.DeviceIdType.MESH
HBM_SPEC = pl.BlockSpec(memory_space=pltpu.HBM)


def _pcall(body, after=None, **kw):
    if after is None:
        return pl.pallas_call(body, **kw)
    n_in = len(kw["in_specs"])
    kw["in_specs"] = list(kw["in_specs"]) + [pl.BlockSpec(memory_space=pl.ANY)]

    def tied(*refs):
        return body(*refs[:n_in], *refs[n_in + 1:])

    call = pl.pallas_call(tied, **kw)
    return lambda *args: call(*args, after)


def _params(*sem):
    return pltpu.CompilerParams(dimension_semantics=sem, vmem_limit_bytes=VMEM_LIMIT)


def _pick(dim, pref):
    t = (min(pref, dim) // 128) * 128
    while t >= 128:
        if dim % t == 0:
            return t
        t -= 128
    return dim


def _rms_bwd_store(r, x_ref, g_ref, dres_ref, o_ref, dg_ref):
    xv = x_ref[...]
    rs = lax.rsqrt(jnp.mean(xv * xv, axis=-1, keepdims=True) + RMS_EPS)
    xh = xv * rs
    dxh = r * g_ref[...]
    o_ref[...] = dres_ref[...] + rs * (dxh - xh * jnp.mean(dxh * xh, axis=-1, keepdims=True))
    part = jnp.sum(r * xh, axis=0, keepdims=True)
    first = pl.program_id(0) == 0

    @pl.when(first)
    def _():
        dg_ref[...] = part

    @pl.when(jnp.logical_not(first))
    def _():
        dg_ref[...] += part


def _d_h2(dgu, w_blocks, x, g, dres, *, name, tm=256, after=None):
    _, S, F = dgu.shape
    n, D, C = w_blocks.shape
    assert n == 4 and F == 2 * C
    nt = (((1,), (1,)), ((), ()))

    def body(dg_ref, du_ref, w_ref, x_ref, g_ref, dres_ref, o_ref, dgn_ref):
        r = None
        for k in range(n):
            a_ref = dg_ref if k < 2 else du_ref
            p = lax.dot_general(a_ref[:, (k % 2) * C:(k % 2 + 1) * C].astype(_CD), w_ref[k].astype(_CD), nt,
                                preferred_element_type=F32)
            r = p if r is None else r + p
        _rms_bwd_store(r, x_ref, g_ref, dres_ref, o_ref, dgn_ref)

    row = pl.BlockSpec((tm, D), lambda i: (i, 0))
    vec = pl.BlockSpec((1, D), lambda i: (0, 0))
    return _pcall(
        body, after, name=name, grid=(S // tm,),
        in_specs=[pl.BlockSpec((None, tm, F), lambda i: (0, i, 0)), pl.BlockSpec((None, tm, F), lambda i: (1, i, 0)),
                  pl.BlockSpec((n, D, C), lambda i: (0, 0, 0)), row, vec, row],
        out_specs=[row, vec], out_shape=[jax.ShapeDtypeStruct((S, D), F32), jax.ShapeDtypeStruct((1, D), F32)],
        compiler_params=_params("arbitrary"))(dgu, dgu, w_blocks, x, g, dres)


def _mm(a, b, *, name, ta=False, tb=False, out_dtype=F32, tm=1024, tn=512, tk=2048, after=None,
        out_blocks=None, b_halves=False, rms_bwd=None, more=()):
    K, M = a.shape if ta else a.shape[::-1]
    b_rows, b_cols = (b.shape[1], 2 * b.shape[2]) if b_halves else b.shape
    if tb:
        N, K2 = b_rows, b_cols
    else:
        K2, N = b_rows, b_cols
    assert K == K2, (a.shape, b.shape)
    tm = _pick(M, tm)
    tn = _pick(out_blocks or N, tn)
    tk = _pick(K, tk)
    nk = K // tk
    dn = (((0 if ta else 1,), (1 if tb else 0,)), ((), ()))
    has_norm = rms_bwd is not None
    if has_norm:
        tn = N
        assert not out_blocks and out_dtype == F32
    assert not more or (nk == 1 and tb and not ta)

    def body(*refs):
        a_ref, b_ref = refs[0], refs[1]
        rest = list(refs[2:])
        more_refs = [(rest.pop(0), rest.pop(0)) for _ in more]
        x_ref, g_ref, dres_ref = (rest.pop(0), rest.pop(0), rest.pop(0)) if has_norm else (None, None, None)
        o_ref = rest.pop(0)
        dg_ref = rest.pop(0) if has_norm else None
        p = lax.dot_general(a_ref[...].astype(_CD), b_ref[...].astype(_CD), dn, preferred_element_type=F32)
        for a2_ref, b2_ref in more_refs:
            p += lax.dot_general(a2_ref[...].astype(_CD), b2_ref[...].astype(_CD), dn, preferred_element_type=F32)

        def finish(r):
            if has_norm:
                _rms_bwd_store(r, x_ref, g_ref, dres_ref, o_ref, dg_ref)
            elif out_blocks:
                o_ref[0] = r.astype(out_dtype)
            else:
                o_ref[...] = r.astype(out_dtype)

        if nk == 1:
            finish(p)
        else:
            acc_ref = rest.pop(0)
            k = pl.program_id(2)

            @pl.when(k == 0)
            def _():
                acc_ref[...] = p

            @pl.when(k > 0)
            def _():
                acc_ref[...] += p

            @pl.when(k == nk - 1)
            def _():
                finish(acc_ref[...])

    a_spec = pl.BlockSpec((tk, tm), lambda i, j, k: (k, i)) if ta else pl.BlockSpec((tm, tk), lambda i, j, k: (i, k))
    if b_halves:
        nb_ = (N // 2) // tn
        b_spec = pl.BlockSpec((None, tk, tn), lambda i, j, k: (j // nb_, k, j % nb_))
    else:
        b_spec = pl.BlockSpec((tn, tk), lambda i, j, k: (j, k)) if tb else pl.BlockSpec((tk, tn), lambda i, j, k: (k, j))
    if out_blocks:
        oper = out_blocks // tn
        o_spec = pl.BlockSpec((1, tm, tn), lambda i, j, k: (j // oper, i, j % oper))
        out_shape = jax.ShapeDtypeStruct((N // out_blocks, M, out_blocks), out_dtype)
    else:
        o_spec = pl.BlockSpec((tm, tn), lambda i, j, k: (i, j))
        out_shape = jax.ShapeDtypeStruct((M, N), out_dtype)
    in_specs, args = [a_spec, b_spec], (a, b)
    for a2, b2 in more:
        assert a2.shape[0] == M and b2.shape == (N, a2.shape[1]), (a2.shape, b2.shape)
        in_specs += [pl.BlockSpec((tm, a2.shape[1]), lambda i, j, k: (i, 0)),
                     pl.BlockSpec((tn, a2.shape[1]), lambda i, j, k: (j, 0))]
        args += (a2, b2)
    out_specs, semantics = o_spec, ("parallel", "parallel", "arbitrary")
    if has_norm:
        vec = pl.BlockSpec((1, N), lambda i, j, k: (0, 0))
        in_specs += [o_spec, vec, o_spec]
        args += tuple(rms_bwd)
        out_specs, out_shape = [o_spec, vec], [out_shape, jax.ShapeDtypeStruct((1, N), F32)]
        semantics = ("arbitrary", "arbitrary", "arbitrary")
    return _pcall(
        body, after, name=name, grid=(M // tm, N // tn, nk), in_specs=in_specs, out_specs=out_specs,
        out_shape=out_shape,
        scratch_shapes=[pltpu.VMEM((tm, tn), F32)] if nk > 1 else [],
        compiler_params=_params(*semantics),
    )(*args)


def _rms_fwd(x, g, *, name, tm=512, after=None):
    S, D = x.shape

    def body(x_ref, g_ref, h_ref):
        xv = x_ref[...]
        r = lax.rsqrt(jnp.mean(xv * xv, axis=-1, keepdims=True) + RMS_EPS)
        h_ref[...] = ((xv * r) * g_ref[...]).astype(h_ref.dtype)

    row = pl.BlockSpec((tm, D), lambda i: (i, 0))
    return _pcall(body, after, name=name, grid=(S // tm,), in_specs=[row, pl.BlockSpec((1, D), lambda i: (0, 0))],
                  out_specs=row, out_shape=jax.ShapeDtypeStruct((S, D), _CD), compiler_params=_params("parallel"))(x, g)


def _ffn_down_loss(act, w_down, x1, g, tgt, *, name, tm=512):
    S, D = x1.shape
    F = act.shape[1]

    def body(a_ref, b_ref, x_ref, g_ref, t_ref, loss_ref, dx_ref, dg_ref):
        xv = x_ref[...] + jnp.dot(a_ref[...].astype(_CD), b_ref[...].astype(_CD), preferred_element_type=F32)
        gv = g_ref[...]
        r = lax.rsqrt(jnp.mean(xv * xv, axis=-1, keepdims=True) + RMS_EPS)
        xh = xv * r
        err = xh * gv - t_ref[...]
        lpart = 0.5 * jnp.sum(jnp.mean(err * err, axis=-1, keepdims=True), axis=0, keepdims=True)
        dy = err * (1.0 / D)
        dxh = dy * gv
        dx_ref[...] = r * (dxh - xh * jnp.mean(dxh * xh, axis=-1, keepdims=True))
        gpart = jnp.sum(dy * xh, axis=0, keepdims=True)

        @pl.when(pl.program_id(0) == 0)
        def _():
            loss_ref[...] = lpart
            dg_ref[...] = gpart

        @pl.when(pl.program_id(0) > 0)
        def _():
            loss_ref[...] += lpart
            dg_ref[...] += gpart

    row = pl.BlockSpec((tm, D), lambda i: (i, 0))
    vec = pl.BlockSpec((1, D), lambda i: (0, 0))
    one = pl.BlockSpec((1, 1), lambda i: (0, 0))
    return _pcall(body, name=name, grid=(S // tm,),
                  in_specs=[pl.BlockSpec((tm, F), lambda i: (i, 0)), pl.BlockSpec((F, D), lambda i: (0, 0)), row, vec, row],
                  out_specs=[one, row, vec],
                  out_shape=[jax.ShapeDtypeStruct((1, 1), F32), jax.ShapeDtypeStruct((S, D), F32),
                             jax.ShapeDtypeStruct((1, D), F32)],
                  compiler_params=_params("arbitrary"))(act, w_down, x1, g, tgt)


def _sigmoid(z):
    return 1.0 / (1.0 + jnp.exp(-z))


def _gated_mix_out(gl, bg, ya, yb, w_out, x, g, *, name, tm=512):
    S, D = ya.shape

    def body(za_ref, zb_ref, ba_ref, bb_ref, ya_ref, yb_ref, w_ref, x_ref, g_ref, m_ref, x1_ref, h_ref):
        ga = _sigmoid(za_ref[...].astype(F32) + ba_ref[...])
        gb = _sigmoid(zb_ref[...].astype(F32) + bb_ref[...])
        merged = (ga * ya_ref[...].astype(F32) + gb * yb_ref[...].astype(F32)).astype(m_ref.dtype)
        m_ref[...] = merged
        x1 = x_ref[...] + jnp.dot(merged, w_ref[...].astype(_CD), preferred_element_type=F32)
        x1_ref[...] = x1
        rs = lax.rsqrt(jnp.mean(x1 * x1, axis=-1, keepdims=True) + RMS_EPS)
        h_ref[...] = ((x1 * rs) * g_ref[...]).astype(h_ref.dtype)

    lo = pl.BlockSpec((tm, D), lambda i: (i, 0))
    hi = pl.BlockSpec((tm, D), lambda i: (i, 1))
    vlo = pl.BlockSpec((1, D), lambda i: (0, 0))
    vhi = pl.BlockSpec((1, D), lambda i: (0, 1))
    whole = pl.BlockSpec((D, D), lambda i: (0, 0))
    return _pcall(body, name=name, grid=(S // tm,), in_specs=[lo, hi, vlo, vhi, lo, lo, whole, lo, vlo],
                  out_specs=[lo, lo, lo],
                  out_shape=[jax.ShapeDtypeStruct((S, D), _CD), jax.ShapeDtypeStruct((S, D), F32),
                             jax.ShapeDtypeStruct((S, D), _CD)],
                  compiler_params=_params("parallel"))(gl, gl, bg, bg, ya, yb, w_out, x, g)


def _gate_bwd(dx1, w_out, gl, bg, ya, yb, *, name, tm=512):
    S, D = ya.shape
    nt = (((1,), (1,)), ((), ()))

    def body(dx_ref, w_ref, za_ref, zb_ref, ba_ref, bb_ref, ya_ref, yb_ref, dya_ref, dyb_ref, dgl_ref, dbg_ref):
        dmv = lax.dot_general(dx_ref[...].astype(_CD), w_ref[...].astype(_CD), nt, preferred_element_type=F32)
        ga = _sigmoid(za_ref[...].astype(F32) + ba_ref[...])
        gb = _sigmoid(zb_ref[...].astype(F32) + bb_ref[...])
        dya_ref[...] = (dmv * ga).astype(dya_ref.dtype)
        dyb_ref[...] = (dmv * gb).astype(dyb_ref.dtype)
        dza = dmv * ya_ref[...].astype(F32) * ga * (1.0 - ga)
        dzb = dmv * yb_ref[...].astype(F32) * gb * (1.0 - gb)
        dgl_ref[:, :D] = dza.astype(dgl_ref.dtype)
        dgl_ref[:, D:] = dzb.astype(dgl_ref.dtype)
        pa = jnp.sum(dza, axis=0, keepdims=True)
        pb = jnp.sum(dzb, axis=0, keepdims=True)

        @pl.when(pl.program_id(0) == 0)
        def _():
            dbg_ref[:, :D] = pa
            dbg_ref[:, D:] = pb

        @pl.when(pl.program_id(0) > 0)
        def _():
            dbg_ref[:, :D] += pa
            dbg_ref[:, D:] += pb

    lo = pl.BlockSpec((tm, D), lambda i: (i, 0))
    hi = pl.BlockSpec((tm, D), lambda i: (i, 1))
    vlo = pl.BlockSpec((1, D), lambda i: (0, 0))
    vhi = pl.BlockSpec((1, D), lambda i: (0, 1))
    wide = pl.BlockSpec((tm, 2 * D), lambda i: (i, 0))
    vwide = pl.BlockSpec((1, 2 * D), lambda i: (0, 0))
    whole = pl.BlockSpec((D, D), lambda i: (0, 0))
    return _pcall(body, name=name, grid=(S // tm,), in_specs=[lo, whole, lo, hi, vlo, vhi, lo, lo],
                  out_specs=[lo, lo, wide, vwide],
                  out_shape=[jax.ShapeDtypeStruct((S, D), _CD), jax.ShapeDtypeStruct((S, D), _CD),
                             jax.ShapeDtypeStruct((S, 2 * D), _CD), jax.ShapeDtypeStruct((1, 2 * D), F32)],
                  compiler_params=_params("arbitrary"))(dx1, w_out, gl, gl, bg, bg, ya, yb)


def _ffn_in_act(h2, w_blocks, *, name, tm=512):
    S, D = h2.shape
    _, _, C = w_blocks.shape

    def body(a_ref, bg_ref, bu_ref, g_ref, u_ref, o_ref):
        av = a_ref[...].astype(_CD)
        gv = jnp.dot(av, bg_ref[0].astype(_CD), preferred_element_type=F32)
        uv = jnp.dot(av, bu_ref[0].astype(_CD), preferred_element_type=F32)
        g_ref[...] = gv.astype(g_ref.dtype)
        u_ref[...] = uv.astype(u_ref.dtype)
        o_ref[...] = (gv * _sigmoid(gv) * uv).astype(o_ref.dtype)

    out = pl.BlockSpec((tm, C), lambda i, j: (i, j))
    shp = jax.ShapeDtypeStruct((S, 2 * C), _CD)
    return _pcall(body, name=name, grid=(S // tm, 2),
                  in_specs=[pl.BlockSpec((tm, D), lambda i, j: (i, 0)), pl.BlockSpec((1, D, C), lambda i, j: (j, 0, 0)),
                            pl.BlockSpec((1, D, C), lambda i, j: (2 + j, 0, 0))],
                  out_specs=[out, out, out], out_shape=[shp, shp, shp],
                  compiler_params=_params("parallel", "arbitrary"))(h2, w_blocks, w_blocks)


def _d_swiglu(dx, w_down, gate, up, *, name, tm=512, tn=1408):
    S, D = dx.shape
    F = w_down.shape[0]
    nt = (((1,), (1,)), ((), ()))

    def body(a_ref, b_ref, g_ref, u_ref, o_ref):
        dv = lax.dot_general(a_ref[...].astype(_CD), b_ref[...].astype(_CD), nt, preferred_element_type=F32)
        gv = g_ref[...].astype(F32)
        sg = _sigmoid(gv)
        o_ref[0] = (dv * u_ref[...].astype(F32) * (sg * (1.0 + gv * (1.0 - sg)))).astype(o_ref.dtype)
        o_ref[1] = (dv * (gv * sg)).astype(o_ref.dtype)

    tile = pl.BlockSpec((tm, tn), lambda i, j: (i, j))
    return _pcall(body, name=name, grid=(S // tm, F // tn),
                  in_specs=[pl.BlockSpec((tm, D), lambda i, j: (i, 0)), pl.BlockSpec((tn, D), lambda i, j: (j, 0)),
                            tile, tile],
                  out_specs=pl.BlockSpec((2, tm, tn), lambda i, j: (0, i, j)),
                  out_shape=jax.ShapeDtypeStruct((2, S, F), _CD),
                  compiler_params=_params("parallel", "arbitrary"))(dx, w_down, gate, up)


def _split3(x):
    hi = x.astype(jnp.bfloat16)
    r1 = x - hi.astype(F32)
    mid = r1.astype(jnp.bfloat16)
    lo = (r1 - mid.astype(F32)).astype(jnp.bfloat16)
    return hi, mid, lo


def _ones_dot_left(ones, x):
    return sum(jnp.dot(ones, p, preferred_element_type=F32) for p in _split3(x))


def _ones_dot_right(x, ones):
    return sum(jnp.dot(p, ones, preferred_element_type=F32) for p in _split3(x))


def _head_sum(x):
    n = x.shape[1]
    r = lax.broadcasted_iota(jnp.int32, (n, n), 0) // HEAD_DIM
    c = lax.broadcasted_iota(jnp.int32, (n, n), 1) // HEAD_DIM
    return _ones_dot_right(x, (r == c).astype(jnp.bfloat16))


def _log_sigmoid(z):
    e = jnp.exp(-jnp.abs(z))
    t = 1.0 + e
    log1p_e = jnp.where(t == 1.0, e, jnp.log(t) * (e / jnp.where(t == 1.0, 1.0, t - 1.0)))
    return jnp.minimum(z, 0.0) - log1p_e


def _fox_cumsum(zf, bf, *, name):
    S, W = zf.shape
    nb = S // 128

    def body(z_ref, b_ref, c_ref):
        tri = (lax.broadcasted_iota(jnp.int32, (128, 128), 0) >= lax.broadcasted_iota(jnp.int32, (128, 128), 1))
        tri = tri.astype(jnp.bfloat16)

        def step(i, carry):
            rows = pl.ds(pl.multiple_of(i * 128, 128), 128)
            lf = _log_sigmoid(z_ref[rows, :] + b_ref[...])
            cb = _ones_dot_left(tri, lf) + carry
            c_ref[rows, :] = cb
            return cb[127:128, :]

        lax.fori_loop(0, nb, step, jnp.zeros((1, W), F32))

    return _pcall(body, name=name, out_shape=jax.ShapeDtypeStruct((S, W), F32),
                  compiler_params=pltpu.CompilerParams(vmem_limit_bytes=VMEM_LIMIT))(zf, bf)


def _fox_cumsum_bwd(dc, zf, bf, *, name):
    S, W = zf.shape
    nb = S // 128

    def body(dc_ref, z_ref, b_ref, dz_ref, db_ref):
        tri = (lax.broadcasted_iota(jnp.int32, (128, 128), 0) <= lax.broadcasted_iota(jnp.int32, (128, 128), 1))
        tri = tri.astype(jnp.bfloat16)

        def step(k, carry):
            tail, acc = carry
            i = nb - 1 - k
            rows = pl.ds(pl.multiple_of(i * 128, 128), 128)
            dlf = _ones_dot_left(tri, dc_ref[rows, :]) + tail
            dz = dlf * _sigmoid(-(z_ref[rows, :] + b_ref[...]))
            dz_ref[rows, :] = dz
            return dlf[0:1, :], acc + jnp.sum(dz, axis=0, keepdims=True)

        _, acc = lax.fori_loop(0, nb, step, (jnp.zeros((1, W), F32), jnp.zeros((1, W), F32)))
        db_ref[...] = acc

    return _pcall(body, name=name,
                  out_shape=[jax.ShapeDtypeStruct((S, W), F32), jax.ShapeDtypeStruct((1, W), F32)],
                  compiler_params=pltpu.CompilerParams(vmem_limit_bytes=VMEM_LIMIT))(dc, zf, bf)


def _proj_dil(h, w_qkv, *, name, tm=1024):
    S, D = h.shape
    tn = DIL_WIDTH

    def body(a_ref, b_ref, *rest):
        outs, acc = rest[:N_DIL_GROUPS], rest[N_DIL_GROUPS]
        prod = jnp.dot(a_ref[...].astype(_CD), b_ref[...].astype(_CD), preferred_element_type=F32)
        for k in range(tn // LANES):
            acc[k] = prod[:, k * LANES:(k + 1) * LANES]
        for g, (_, d) in enumerate(DIL_PAIRS):
            for half in range(DIL_OUT // LANES):
                k = g * (DIL_OUT // LANES) + half
                cols = slice(half * LANES, (half + 1) * LANES)
                for r in range(d):
                    rows = pl.ds(r, tm // d, stride=d) if d > 1 else slice(None)
                    outs[g][0, r, :, cols] = acc[k, rows, :].astype(outs[g].dtype)

    out_specs = [pl.BlockSpec((1, d, tm // d, DIL_OUT), lambda i, j: (j, 0, i, 0)) for _, d in DIL_PAIRS]
    out_shape = [jax.ShapeDtypeStruct((3, d, S // d, DIL_OUT), _CD) for _, d in DIL_PAIRS]
    outs = _pcall(body, name=name, grid=(S // tm, 3),
                  in_specs=[pl.BlockSpec((tm, D), lambda i, j: (i, 0)), pl.BlockSpec((D, tn), lambda i, j: (0, j))],
                  out_specs=out_specs, out_shape=out_shape, scratch_shapes=[pltpu.VMEM((tn // LANES, tm, LANES), F32)],
                  compiler_params=_params("parallel", "arbitrary"))(h, w_qkv)
    return [o.reshape(3, S, DIL_OUT) for o in outs]


def _dil_start(block, S, dilation):
    sub = S // dilation
    u0 = block * DIL_W
    return (u0 % sub) * dilation + u0 // sub


def _dil_slopes(group):
    h = np.arange(1, N_DIL_GROUPS * DIL_HEADS + 1, dtype=np.float32)
    s = (np.float32(2.0) ** (np.float32(-8.0) * h / np.float32(N_DIL_GROUPS * DIL_HEADS))).astype(np.float32)
    return [float(v) for v in s.reshape(N_DIL_GROUPS, DIL_HEADS)[group]]


def _dil_tiles(i, n, blocks_per_seq):
    qi = lax.broadcasted_iota(jnp.int32, (DIL_W, 2 * DIL_W), 0)
    kj = lax.broadcasted_iota(jnp.int32, (DIL_W, 2 * DIL_W), 1)
    rel = qi + DIL_W - kj
    first = ((4 * n + i) % blocks_per_seq) == 0
    valid = jnp.logical_and(jnp.logical_and(rel >= 0, rel <= DIL_W), jnp.logical_or(kj >= DIL_W, jnp.logical_not(first)))
    return valid, rel.astype(F32)


def _dil_window(cur_ref, prev_ref, i, cols):
    if i > 0:
        return cur_ref[(i - 1) * DIL_W:(i + 1) * DIL_W, cols]
    return jnp.concatenate([prev_ref[:, cols], cur_ref[:DIL_W, cols]], axis=0)


CHUNK = 4 * DIL_W


def _dil_rows(block, S, dilation):
    start = _dil_start(block, S, dilation)
    return pl.ds(start, DIL_W, stride=dilation) if dilation > 1 else pl.ds(start, DIL_W)


def SPLIT(S):
    return (DIL_OUT // LANES, S, LANES)


def _dil_fwd(qkv, group, *, name):
    S = qkv.shape[1]
    dilation = DIL_PAIRS[group][1]
    bps = (S // dilation) // DIL_W
    slopes = _dil_slopes(group)
    nt = (((1,), (1,)), ((), ()))

    def body(q_ref, k_ref, v_ref, kp_ref, vp_ref, on_ref, ln_ref, o_ref, l_ref):
        n = pl.program_id(0)
        for i in range(4):
            valid, rel = _dil_tiles(i, n, bps)
            rows = slice(i * DIL_W, (i + 1) * DIL_W)
            for h in range(DIL_HEADS):
                cols = slice(h * HEAD_DIM, (h + 1) * HEAD_DIM)
                qh = q_ref[rows, cols]
                k2, v2 = _dil_window(k_ref, kp_ref, i, cols), _dil_window(v_ref, vp_ref, i, cols)
                s = lax.dot_general(qh, k2, nt, preferred_element_type=F32) * ATTN_SCALE - (slopes[h] * dilation) * rel
                s = jnp.where(valid, s, NEG_INF)
                m = jnp.max(s, axis=-1, keepdims=True)
                p = jnp.exp(s - m)
                den = jnp.sum(p, axis=-1, keepdims=True)
                acc = jnp.dot(p.astype(_CD), v2, preferred_element_type=F32)
                o_ref[rows, cols] = acc / den
                l_ref[rows, cols] = jnp.broadcast_to(m + jnp.log(den), (DIL_W, HEAD_DIM))
        for i in range(4):
            rows = slice(i * DIL_W, (i + 1) * DIL_W)
            nat = _dil_rows(4 * n + i, S, dilation)
            for half in range(DIL_OUT // LANES):
                cols = slice(half * LANES, (half + 1) * LANES)
                on_ref[half, nat, :] = o_ref[rows, cols]
                ln_ref[half, nat, :] = l_ref[rows, cols]

    def cur(which):
        return pl.BlockSpec((None, CHUNK, DIL_OUT), lambda n: (which, n, 0))

    def prev(which):
        return pl.BlockSpec((None, DIL_W, DIL_OUT), lambda n: (which, jnp.maximum(4 * n - 1, 0), 0))

    whole = pl.BlockSpec(SPLIT(S), lambda n: (0, 0, 0))
    return _pcall(body, name=name, grid=(S // CHUNK,), in_specs=[cur(0), cur(1), cur(2), prev(1), prev(2)],
                  out_specs=[whole, whole],
                  out_shape=[jax.ShapeDtypeStruct(SPLIT(S), F32), jax.ShapeDtypeStruct(SPLIT(S), F32)],
                  scratch_shapes=[pltpu.VMEM((CHUNK, DIL_OUT), F32), pltpu.VMEM((CHUNK, DIL_OUT), F32)],
                  compiler_params=_params("arbitrary"))(qkv, qkv, qkv, qkv, qkv)


STAT_OFFSET = HEAD_DIM // 2


def _dil_bwd(qkv, stats, do, group, *, name):
    S = qkv.shape[1]
    dilation = DIL_PAIRS[group][1]
    bps = (S // dilation) // DIL_W
    slopes = _dil_slopes(group)
    nchunk = S // CHUNK
    nt = (((1,), (1,)), ((), ()))
    tn = (((0,), (0,)), ((), ()))

    def body(q_ref, k_ref, v_ref, kp_ref, vp_ref, ln_ref, don_ref, dqn_ref, dkn_ref, dvn_ref,
             dk_s, dv_s, l_ref, do_ref, dq_ref):
        step = pl.program_id(0)
        n = nchunk - 1 - step
        for i in range(4):
            rows = slice(i * DIL_W, (i + 1) * DIL_W)
            nat = _dil_rows(4 * n + i, S, dilation)
            for half in range(DIL_OUT // LANES):
                cols = slice(half * LANES, (half + 1) * LANES)
                l_ref[rows, cols] = ln_ref[half, nat, :]
                do_ref[rows, cols] = don_ref[half, nat, :]

        @pl.when(step == 0)
        def _():
            dk_s[:, CHUNK:] = jnp.zeros((DIL_OUT, DIL_W), F32)
            dv_s[:, CHUNK:] = jnp.zeros((DIL_OUT, DIL_W), F32)

        dk_s[:, :CHUNK] = jnp.zeros((DIL_OUT, CHUNK), F32)
        dv_s[:, :CHUNK] = jnp.zeros((DIL_OUT, CHUNK), F32)
        for i in range(4):
            valid, rel = _dil_tiles(i, n, bps)
            rows = slice(i * DIL_W, (i + 1) * DIL_W)
            window = slice(i * DIL_W, (i + 2) * DIL_W)
            for h in range(DIL_HEADS):
                cols = slice(h * HEAD_DIM, (h + 1) * HEAD_DIM)
                qh = q_ref[rows, cols]
                k2, v2 = _dil_window(k_ref, kp_ref, i, cols), _dil_window(v_ref, vp_ref, i, cols)
                lh = l_ref[rows, h * HEAD_DIM:h * HEAD_DIM + 1]
                shift = l_ref[rows, h * HEAD_DIM + STAT_OFFSET:h * HEAD_DIM + STAT_OFFSET + 1]
                s = lax.dot_general(qh, k2, nt, preferred_element_type=F32) * ATTN_SCALE - (slopes[h] * dilation) * rel
                p = jnp.exp(jnp.where(valid, s, NEG_INF) - lh)
                dob = do_ref[rows, cols].astype(_CD)
                ds = p * (lax.dot_general(dob, v2, nt, preferred_element_type=F32) + shift)
                dsb = (ds * ATTN_SCALE).astype(_CD)
                dq_ref[rows, cols] = jnp.dot(dsb, k2, preferred_element_type=F32)
                dk_s[cols, window] += lax.dot_general(qh, dsb, tn, preferred_element_type=F32)
                dv_s[cols, window] += lax.dot_general(dob, p.astype(_CD), tn, preferred_element_type=F32)
        for i in range(4):
            rows = slice(i * DIL_W, (i + 1) * DIL_W)
            done = slice((i + 1) * DIL_W, (i + 2) * DIL_W)
            nat = _dil_rows(4 * n + i, S, dilation)
            dkb, dvb = dk_s[:, done].T, dv_s[:, done].T
            for half in range(DIL_OUT // LANES):
                cols = slice(half * LANES, (half + 1) * LANES)
                dqn_ref[half, nat, :] = dq_ref[rows, cols]
                dkn_ref[half, nat, :] = dkb[:, cols]
                dvn_ref[half, nat, :] = dvb[:, cols]
        dk_s[:, CHUNK:] = dk_s[:, :DIL_W]
        dv_s[:, CHUNK:] = dv_s[:, :DIL_W]

    def cur(which):
        return pl.BlockSpec((None, CHUNK, DIL_OUT), lambda s: (which, nchunk - 1 - s, 0))

    def prev(which):
        return pl.BlockSpec((None, DIL_W, DIL_OUT), lambda s: (which, jnp.maximum(4 * (nchunk - 1 - s) - 1, 0), 0))

    whole = pl.BlockSpec(SPLIT(S), lambda s: (0, 0, 0))
    shp = jax.ShapeDtypeStruct(SPLIT(S), F32)
    tile = pltpu.VMEM((CHUNK, DIL_OUT), F32)
    return _pcall(body, name=name, grid=(nchunk,),
                  in_specs=[cur(0), cur(1), cur(2), prev(1), prev(2), whole, whole],
                  out_specs=[whole, whole, whole], out_shape=[shp, shp, shp],
                  scratch_shapes=[pltpu.VMEM((DIL_OUT, CHUNK + DIL_W), F32), pltpu.VMEM((DIL_OUT, CHUNK + DIL_W), F32),
                                  tile, tile, tile],
                  compiler_params=pltpu.CompilerParams(dimension_semantics=("arbitrary",),
                                                       vmem_limit_bytes=VMEM_LIMIT_RESIDENT))(
        qkv, qkv, qkv, qkv, qkv, stats, do)


def _dil_mix_fwd(os_, ls_, *, name, tm=512):
    nh, S, _ = os_[0].shape

    def body(o0, o1, o2, l0, l1, l2, out_ref):
        for half in range(nh):
            ls = [l0[half], l1[half], l2[half]]
            m = jnp.maximum(jnp.maximum(ls[0], ls[1]), ls[2])
            es = [jnp.exp(l - m) for l in ls]
            den = es[0] + es[1] + es[2]
            mixed = (es[0] * o0[half] + es[1] * o1[half] + es[2] * o2[half]) / den
            out_ref[:, half * LANES:(half + 1) * LANES] = mixed.astype(out_ref.dtype)

    halves = pl.BlockSpec((nh, tm, LANES), lambda i: (0, i, 0))
    row = pl.BlockSpec((tm, nh * LANES), lambda i: (i, 0))
    return _pcall(body, name=name, grid=(S // tm,), in_specs=[halves] * 6, out_specs=row,
                  out_shape=jax.ShapeDtypeStruct((S, nh * LANES), _CD), compiler_params=_params("parallel"))(*os_, *ls_)


def _dil_mix_bwd(doa, os_, ls_, *, name, tm=512, after=None):
    nh, S, _ = os_[0].shape

    def body(d_ref, o0, o1, o2, l0, l1, l2, do0, do1, do2, st0, st1, st2):
        first = lax.broadcasted_iota(jnp.int32, (tm, LANES), 1) % HEAD_DIM < STAT_OFFSET
        for half in range(nh):
            dv = d_ref[:, half * LANES:(half + 1) * LANES]
            ls = [l0[half], l1[half], l2[half]]
            m = jnp.maximum(jnp.maximum(ls[0], ls[1]), ls[2])
            es = [jnp.exp(l - m) for l in ls]
            den = es[0] + es[1] + es[2]
            al = [e / den for e in es]
            da = [_head_sum(dv * o[half]) for o in (o0, o1, o2)]
            mean = al[0] * da[0] + al[1] * da[1] + al[2] * da[2]
            for a, l, do_ref, st_ref in zip(al, ls, (do0, do1, do2), (st0, st1, st2)):
                do_ref[half] = a * dv
                st_ref[half] = jnp.where(first, l, -a * mean)

    halves = pl.BlockSpec((nh, tm, LANES), lambda i: (0, i, 0))
    row = pl.BlockSpec((tm, nh * LANES), lambda i: (i, 0))
    shp = jax.ShapeDtypeStruct((nh, S, LANES), F32)
    return _pcall(body, after, name=name, grid=(S // tm,), in_specs=[row] + [halves] * 6, out_specs=[halves] * 6,
                  out_shape=[shp] * 6, compiler_params=_params("parallel"))(doa, *os_, *ls_)


FOX_T = 512


PACK = 2 * HEAD_DIM
HEAD_PAIRS = N_FOX_HEADS // 2
FOX_HPS = 8
Q_BLOCK0 = 0
K_BLOCK0 = FOX_WIDTH // PACK
V_BLOCK0 = 2 * FOX_WIDTH // PACK


def _pieces(x):
    hi = x.astype(jnp.bfloat16).astype(F32)
    r = x - hi
    mid = r.astype(jnp.bfloat16).astype(F32)
    lo = (r - mid).astype(jnp.bfloat16).astype(F32)
    return [hi, mid, lo]


def _extras(first, second, rows):
    lane = lax.broadcasted_iota(jnp.int32, (rows, HEAD_DIM), 1)
    out = jnp.zeros((rows, HEAD_DIM), F32)
    for base, triple in ((0, first), (3, second)):
        if all(isinstance(v, float) for v in triple) and len(set(triple)) == 1:
            if triple[0] != 0.0:
                out = jnp.where(jnp.logical_and(lane >= base, lane < base + 3), triple[0], out)
        else:
            for idx, val in enumerate(triple):
                out = jnp.where(lane == base + idx, val, out)
    return out


def _head_column(c, h):
    lane = lax.broadcasted_iota(jnp.int32, c.shape, 1)
    return jnp.sum(jnp.where(lane == h, c, 0.0), axis=1, keepdims=True)


ONES3 = [1.0, 1.0, 1.0]
ZEROS3 = [0.0, 0.0, 0.0]


def _fox_pack_fwd(qkv, c, *, name, tm=1024):
    S = qkv.shape[0]

    def body(q_ref, k_ref, v_ref, c_ref, qo_ref, ko_ref, vo_ref):
        hp = pl.program_id(1)
        cv = c_ref[...]
        v_extras = jnp.where(lax.broadcasted_iota(jnp.int32, (tm, HEAD_DIM), 1) < 3, 1.0, 0.0).astype(vo_ref.dtype)
        for hh in range(2):
            ch = _pieces(_head_column(cv, 2 * hp + hh))
            src = slice(hh * HEAD_DIM, (hh + 1) * HEAD_DIM)
            lo = slice(hh * PACK, hh * PACK + HEAD_DIM)
            hi = slice(hh * PACK + HEAD_DIM, (hh + 1) * PACK)
            qo_ref[:, lo] = (q_ref[:, src].astype(F32) * ATTN_SCALE).astype(qo_ref.dtype)
            qo_ref[:, hi] = _extras(ch, ONES3, tm).astype(qo_ref.dtype)
            ko_ref[:, lo] = k_ref[:, src]
            ko_ref[:, hi] = _extras(ONES3, [-p for p in ch], tm).astype(ko_ref.dtype)
            vo_ref[:, lo] = v_ref[:, src]
            vo_ref[:, hi] = v_extras

    def src(block0):
        return pl.BlockSpec((tm, PACK), lambda i, hp: (i, block0 + hp))

    out = pl.BlockSpec((tm, 2 * PACK), lambda i, hp: (i, hp))
    shp = jax.ShapeDtypeStruct((S, N_FOX_HEADS * PACK), _CD)
    return _pcall(body, name=name, grid=(S // tm, HEAD_PAIRS),
                  in_specs=[src(Q_BLOCK0), src(K_BLOCK0), src(V_BLOCK0), pl.BlockSpec((tm, PACK), lambda i, hp: (i, 0))],
                  out_specs=[out, out, out], out_shape=[shp, shp, shp],
                  compiler_params=_params("parallel", "parallel"))(qkv, qkv, qkv, c)


def _fox_fwd(qp, kp, vp, *, name):
    S = qp.shape[0]
    nt = S // FOX_T
    nt_dims = (((1,), (1,)), ((), ()))
    tn_dims = (((0,), (0,)), ((), ()))

    def body(i_tab, j_tab, q_ref, k_ref, v_ref, o_ref, l_ref, m_s, acc_s):
        t = pl.program_id(1)
        i, j = i_tab[t], j_tab[t]

        @pl.when(j == 0)
        def _():
            m_s[...] = jnp.full((FOX_HPS, 1, FOX_T), NEG_INF, F32)
            acc_s[...] = jnp.zeros((FOX_HPS, PACK, FOX_T), F32)

        def tile(diagonal):
            for hh in range(FOX_HPS):
                cols = slice(hh * PACK, (hh + 1) * PACK)
                st = lax.dot_general(k_ref[:, cols], q_ref[:, cols], nt_dims, preferred_element_type=F32)
                if diagonal:
                    key = lax.broadcasted_iota(jnp.int32, (FOX_T, FOX_T), 0)
                    qry = lax.broadcasted_iota(jnp.int32, (FOX_T, FOX_T), 1)
                    st = jnp.where(key <= qry, st, NEG_INF)
                m_old = m_s[hh]
                m_new = jnp.maximum(m_old, jnp.max(st, axis=0, keepdims=True))
                pt = jnp.exp(st - m_new)
                acc_s[hh] = jnp.exp(m_old - m_new) * acc_s[hh] + lax.dot_general(
                    v_ref[:, cols], pt.astype(_CD), tn_dims, preferred_element_type=F32)
                m_s[hh] = m_new

        @pl.when(j < i)
        def _():
            tile(False)

        @pl.when(j == i)
        def _():
            tile(True)
            for hh in range(FOX_HPS):
                acc = acc_s[hh]
                den = acc[HEAD_DIM:HEAD_DIM + 1, :]
                cols = slice(hh * HEAD_DIM, (hh + 1) * HEAD_DIM)
                o_ref[:, cols] = (acc[:HEAD_DIM, :] / den).T
                l_ref[:, cols] = jnp.broadcast_to(m_s[hh] + jnp.log(den), (HEAD_DIM, FOX_T)).T

    pairs = [(i, j) for i in range(nt) for j in range(i + 1)]
    i_tab = jnp.asarray([p[0] for p in pairs], jnp.int32)
    j_tab = jnp.asarray([p[1] for p in pairs], jnp.int32)
    qs = pl.BlockSpec((FOX_T, FOX_HPS * PACK), lambda hp, t, it, jt: (it[t], hp))
    ks = pl.BlockSpec((FOX_T, FOX_HPS * PACK), lambda hp, t, it, jt: (jt[t], hp))
    os_ = pl.BlockSpec((FOX_T, FOX_HPS * HEAD_DIM), lambda hp, t, it, jt: (it[t], hp))
    shp = jax.ShapeDtypeStruct((S, FOX_WIDTH), F32)
    grid_spec = pltpu.PrefetchScalarGridSpec(
        num_scalar_prefetch=2, grid=(N_FOX_HEADS // FOX_HPS, len(pairs)), in_specs=[qs, ks, ks], out_specs=[os_, os_],
        scratch_shapes=[pltpu.VMEM((FOX_HPS, 1, FOX_T), F32), pltpu.VMEM((FOX_HPS, PACK, FOX_T), F32)])
    return _pcall(body, name=name, grid_spec=grid_spec, out_shape=[shp, shp],
                  compiler_params=_params("parallel", "arbitrary"))(i_tab, j_tab, qp, kp, vp)


def _fox_pack_bwd(qkv, c, o, lse, do, *, name, tm=1024, after=None):
    S = qkv.shape[0]

    def body(q_ref, c_ref, o_ref, l_ref, do_ref, qo_ref, do_out_ref):
        hp = pl.program_id(1)
        cv = c_ref[...]
        for hh in range(2):
            src = slice(hh * HEAD_DIM, (hh + 1) * HEAD_DIM)
            lo = slice(hh * PACK, hh * PACK + HEAD_DIM)
            hi = slice(hh * PACK + HEAD_DIM, (hh + 1) * PACK)
            shift = _head_column(cv, 2 * hp + hh) - l_ref[:, hh * HEAD_DIM:hh * HEAD_DIM + 1]
            dov = do_ref[:, src]
            dsum = jnp.sum(dov * o_ref[:, src], axis=-1, keepdims=True)
            qo_ref[:, lo] = (q_ref[:, src].astype(F32) * ATTN_SCALE).astype(qo_ref.dtype)
            qo_ref[:, hi] = _extras(_pieces(shift), ONES3, tm).astype(qo_ref.dtype)
            do_out_ref[:, lo] = dov.astype(do_out_ref.dtype)
            do_out_ref[:, hi] = _extras(_pieces(-dsum), ZEROS3, tm).astype(do_out_ref.dtype)

    pair = pl.BlockSpec((tm, PACK), lambda i, hp: (i, hp))
    out = pl.BlockSpec((tm, 2 * PACK), lambda i, hp: (i, hp))
    shp = jax.ShapeDtypeStruct((S, N_FOX_HEADS * PACK), _CD)
    return _pcall(body, after, name=name, grid=(S // tm, HEAD_PAIRS),
                  in_specs=[pl.BlockSpec((tm, PACK), lambda i, hp: (i, Q_BLOCK0 + hp)),
                            pl.BlockSpec((tm, PACK), lambda i, hp: (i, 0)), pair, pair, pair],
                  out_specs=[out, out], out_shape=[shp, shp],
                  compiler_params=_params("parallel", "parallel"))(qkv, c, o, lse, do)


def _fox_bwd(qp, kp, vp, dop, *, name):
    S = qp.shape[0]
    nt = S // FOX_T
    nt_dims = (((1,), (1,)), ((), ()))
    tn_dims = (((0,), (0,)), ((), ()))

    def body(i_tab, j_tab, q_ref, k_ref, v_ref, do_ref, dq_ref, dk_ref, dv_ref, dc_ref, dr_ref,
             dq_s, dk_s, dv_s, dc_s, dr_s):
        t = pl.program_id(1)
        i, j = i_tab[t], j_tab[t]

        @pl.when(t == 0)
        def _():
            dq_s[...] = jnp.zeros((S, FOX_HPS * PACK), F32)
            dr_s[...] = jnp.zeros((FOX_HPS, 1, S), F32)

        @pl.when(i == j)
        def _():
            dk_s[...] = jnp.zeros((FOX_T, FOX_HPS * PACK), F32)
            dv_s[...] = jnp.zeros((FOX_T, FOX_HPS * PACK), F32)
            dc_s[...] = jnp.zeros((FOX_HPS, FOX_T, 1), F32)

        def tile(diagonal):
            rows = pl.ds(pl.multiple_of(i * FOX_T, FOX_T), FOX_T)
            for hh in range(FOX_HPS):
                cols = slice(hh * PACK, (hh + 1) * PACK)
                qv, kv, vv, dov = q_ref[:, cols], k_ref[:, cols], v_ref[:, cols], do_ref[:, cols]
                pt = jnp.exp(lax.dot_general(kv, qv, nt_dims, preferred_element_type=F32))
                if diagonal:
                    key = lax.broadcasted_iota(jnp.int32, (FOX_T, FOX_T), 0)
                    qry = lax.broadcasted_iota(jnp.int32, (FOX_T, FOX_T), 1)
                    pt = jnp.where(key <= qry, pt, 0.0)
                dst = pt * lax.dot_general(vv, dov, nt_dims, preferred_element_type=F32)
                dsb = dst.astype(_CD)
                dc_s[hh] += jnp.sum(dst, axis=1, keepdims=True)
                dr_s[hh, :, rows] += jnp.sum(dst, axis=0, keepdims=True)
                dv_s[:, cols] += jnp.dot(pt.astype(_CD), dov, preferred_element_type=F32)
                dk_s[:, cols] += jnp.dot(dsb, qv, preferred_element_type=F32)
                dq_s[rows, cols] += lax.dot_general(dsb, kv, tn_dims, preferred_element_type=F32)

        @pl.when(i > j)
        def _():
            tile(False)

        @pl.when(i == j)
        def _():
            tile(True)

        @pl.when(i == nt - 1)
        def _():
            for hh in range(FOX_HPS):
                src = slice(hh * PACK, hh * PACK + HEAD_DIM)
                dst_cols = slice(hh * HEAD_DIM, (hh + 1) * HEAD_DIM)
                dk_ref[:, dst_cols] = dk_s[:, src].astype(dk_ref.dtype)
                dv_ref[:, dst_cols] = dv_s[:, src].astype(dv_ref.dtype)
                dc_ref[:, dst_cols] = jnp.broadcast_to(dc_s[hh], (FOX_T, HEAD_DIM))

        @pl.when(t == len(pairs) - 1)
        def _():
            for hh in range(FOX_HPS):
                dq_ref[:, hh * HEAD_DIM:(hh + 1) * HEAD_DIM] = (
                    dq_s[:, hh * PACK:hh * PACK + HEAD_DIM] * ATTN_SCALE).astype(dq_ref.dtype)
            dr_ref[...] = dr_s[...]

    pairs = [(i, j) for j in range(nt) for i in range(j, nt)]
    i_tab = jnp.asarray([p[0] for p in pairs], jnp.int32)
    j_tab = jnp.asarray([p[1] for p in pairs], jnp.int32)
    wide, narrow = FOX_HPS * PACK, FOX_HPS * HEAD_DIM
    qs = pl.BlockSpec((FOX_T, wide), lambda hp, t, it, jt: (it[t], hp))
    ks = pl.BlockSpec((FOX_T, wide), lambda hp, t, it, jt: (jt[t], hp))
    whole = pl.BlockSpec((S, narrow), lambda hp, t, it, jt: (0, hp))
    cs = pl.BlockSpec((FOX_T, narrow), lambda hp, t, it, jt: (jt[t], hp))
    rs = pl.BlockSpec((FOX_HPS, 1, S), lambda hp, t, it, jt: (hp, 0, 0))
    shp = jax.ShapeDtypeStruct((S, FOX_WIDTH), _CD)
    grid_spec = pltpu.PrefetchScalarGridSpec(
        num_scalar_prefetch=2, grid=(N_FOX_HEADS // FOX_HPS, len(pairs)), in_specs=[qs, ks, ks, qs],
        out_specs=[whole, cs, cs, cs, rs],
        scratch_shapes=[pltpu.VMEM((S, wide), F32), pltpu.VMEM((FOX_T, wide), F32),
                        pltpu.VMEM((FOX_T, wide), F32), pltpu.VMEM((FOX_HPS, FOX_T, 1), F32),
                        pltpu.VMEM((FOX_HPS, 1, S), F32)])
    return _pcall(body, name=name, grid_spec=grid_spec,
                  out_shape=[shp, shp, shp, jax.ShapeDtypeStruct((S, FOX_WIDTH), F32),
                             jax.ShapeDtypeStruct((N_FOX_HEADS, 1, S), F32)],
                  compiler_params=_params("parallel", "arbitrary"))(i_tab, j_tab, qp, kp, vp, dop)


def _layer_step(x, tgt, w, p, late_weights=None, grad_sink=None, after=None, first_weights=None):
    S = x.shape[0]
    after_norm, after_proj = after if after is not None else (None, None)
    h = _rms_fwd(x, p["norm_mix_g"], name="rms_mix", after=after_norm)
    if first_weights is not None:
        w = {**w, **first_weights(h)}
    qkv = _mm(h, w["qkv"][:, 3 * DIL_WIDTH:], name="proj_fox", out_dtype=_CD, tn=768, tm=2048, after=after_proj)
    dil_qkv = _proj_dil(h, w["qkv"], name="proj_dil")
    zf = _mm(h, w["f"], name="proj_f")
    gl = _mm(h, w["g"], name="proj_gate", tn=1024, out_dtype=_CD)

    dil_o, dil_l = [], []
    for g in range(N_DIL_GROUPS):
        og, lg = _dil_fwd(dil_qkv[g], g, name=f"dil_fwd{g}")
        dil_o.append(og), dil_l.append(lg)
    o_a = _dil_mix_fwd(dil_o, dil_l, name="dil_mix")

    c = _fox_cumsum(zf, p["b_fgt"], name="fox_cumsum")
    fqp, fkp, fvp = _fox_pack_fwd(qkv, c, name="fox_pack")
    o_b, flse = _fox_fwd(fqp, fkp, fvp, name="fox_fwd")

    if late_weights is not None:
        w = {**w, **late_weights(o_b)}
    y_a = _mm(o_a, w["dil_out"], name="y_a", tn=1024, out_dtype=_CD)
    y_b = _mm(o_b, w["fox_out"], name="y_b", tn=1024, out_dtype=_CD)
    merged, x1, h2 = _gated_mix_out(gl, p["b_gate"], y_a, y_b, w["out"], x, p["norm_ffn_g"], name="mix_out")
    gate, up, act = _ffn_in_act(h2, w["ffn_in"], name="ffn_in")
    loss, dx2, dg_final = _ffn_down_loss(act, w["ffn_down"], x1, p["norm_final_g"], tgt, name="ffn_down_loss")

    gw_ffn_down = _mm(act, dx2, name="gw_ffn_down", ta=True, out_dtype=_CD, tm=1408)
    dgu = _d_swiglu(dx2, w["ffn_down"], gate, up, name="d_swiglu")
    gw_ffn_in = _mm(h2, dgu, name="gw_ffn_in", ta=True, out_dtype=_CD, tn=1408, out_blocks=1408, b_halves=True)
    sink = grad_sink if grad_sink is not None else (lambda group, grads: None)
    tok = sink("ffn", dict(ffn_in=gw_ffn_in, ffn_down=gw_ffn_down))
    dx1, dg_ffn = _d_h2(dgu, w["ffn_in"], x1, p["norm_ffn_g"], dx2, name="d_h2", after=tok)

    gw_out = _mm(merged, dx1, name="gw_out", ta=True, out_dtype=_CD)
    dy_a, dy_b, dgl, db_gate = _gate_bwd(dx1, w["out"], gl, p["b_gate"], y_a, y_b, name="gate_bwd")
    do_a = _mm(dy_a, w["dil_out"], name="d_o_a", tb=True)
    gw_dil_out = _mm(o_a, dy_a, name="gw_dil_out", ta=True, out_dtype=_CD, tn=1024)
    do_b = _mm(dy_b, w["fox_out"], name="d_o_b", tb=True)
    gw_fox_out = _mm(o_b, dy_b, name="gw_fox_out", ta=True, out_dtype=_CD, tn=1024)
    tok = sink("mix", dict(dil_out=gw_dil_out, fox_out=gw_fox_out, out=gw_out))

    bqp, bdop = _fox_pack_bwd(qkv, c, o_b, flse, do_b, name="fox_pack_bwd", after=tok)
    dqp, dkp, dvp, dck, dcq = _fox_bwd(bqp, fkp, fvp, bdop, name="fox_bwd")
    dc = dcq[:, 0, :].T - dck.reshape(S, N_FOX_HEADS, HEAD_DIM)[:, :, 0]
    dc = jnp.pad(dc, ((0, 0), (0, F_PAD - N_FOX_HEADS)))
    dzf, db_fgt = _fox_cumsum_bwd(dc, zf, p["b_fgt"], name="fox_cumsum_bwd")

    douts = _dil_mix_bwd(do_a, dil_o, dil_l, name="dil_mix_bwd", after=tok)
    dqs, dks, dvs = [], [], []
    for g in range(N_DIL_GROUPS):
        dq, dk, dv = _dil_bwd(dil_qkv[g], douts[3 + g], douts[g], g, name=f"dil_bwd{g}")
        for parts, t in ((dqs, dq), (dks, dk), (dvs, dv)):
            parts.extend([t[0].astype(_CD), t[1].astype(_CD)])
    dqkv = jnp.concatenate(dqs + dks + dvs + [dqp, dkp, dvp], axis=1)

    gw_qkv = _mm(h, dqkv, name="gw_qkv", ta=True, out_dtype=_CD, tn=768, tk=S)
    gw_g = _mm(h, dgl, name="gw_gate", ta=True, out_dtype=_CD, tk=S)
    gw_f = _mm(h, dzf, name="gw_f", ta=True, out_dtype=_CD)
    tok = sink("in", dict(qkv=gw_qkv, f=gw_f, g=gw_g))
    dx, dg_mix = _mm(dqkv, w["qkv"], name="d_h", tb=True, tk=QKV_COLS, tm=256, more=((dgl, w["g"]), (dzf, w["f"])),
                     rms_bwd=(x, p["norm_mix_g"], dx1), after=tok)

    gw = dict(qkv=gw_qkv, f=gw_f, g=gw_g, dil_out=gw_dil_out, fox_out=gw_fox_out, out=gw_out, ffn_in=gw_ffn_in,
              ffn_down=gw_ffn_down)
    small = dict(norm_mix_g=dg_mix, b_fgt=db_fgt, b_gate=db_gate, norm_ffn_g=dg_ffn, norm_final_g=dg_final)
    return loss, dx, gw, small


def _position():
    return lax.axis_index("x"), lax.axis_index("y"), lax.axis_index("c")


def _other_chips(x, y):
    return [(1 - x, y), (x, 1 - y), (1 - x, 1 - y)]


ROW_TILE = 16


def _row_chunks(rows, want=8):
    n = want
    while n > 1 and rows % (n * ROW_TILE):
        n //= 2
    return n


SEM_SPEC = pl.BlockSpec(memory_space=pltpu.SEMAPHORE)
ANY_SPEC = pl.BlockSpec(memory_space=pl.ANY)
DATAFLOW = pltpu.SideEffectType.DATAFLOW_SIDE_EFFECTING


def _in_hbm(a):
    return pltpu.with_memory_space_constraint(a, pltpu.HBM)


def _split_copy_start(srcs, land_shapes, copies, after, *, name):
    n, m = len(srcs), len(land_shapes)

    def body(*refs):
        src_refs, land_refs = refs[:n], refs[n:n + m]
        send_sems, recv_sems = refs[n + m + 1], refs[n + m + 2]
        token = refs[-1]
        x, y, c = _position()
        for k, (src, dst, peer) in enumerate(copies(x, y, c, src_refs, land_refs)):
            pltpu.make_async_remote_copy(src_ref=src, dst_ref=dst, send_sem=send_sems.at[k], recv_sem=recv_sems.at[k],
                                         device_id=peer, device_id_type=MESH).start()
        token[...] = jnp.zeros_like(token)

    lands = [lax.empty(s.shape, s.dtype) for s in land_shapes]
    count = len(copies(0, 0, 0, srcs, lands))
    out = _pcall(
        body, name=name,
        out_shape=(pltpu.SemaphoreType.DMA((count,)), pltpu.SemaphoreType.DMA((count,)),
                   *[pltpu.HBM(s.shape, s.dtype) for s in srcs], *[pltpu.HBM(s.shape, s.dtype) for s in land_shapes],
                   jax.ShapeDtypeStruct((8, 128), F32)),
        in_specs=[HBM_SPEC] * (n + m) + [ANY_SPEC],
        out_specs=(SEM_SPEC, SEM_SPEC, *[HBM_SPEC] * (n + m), pl.BlockSpec(memory_space=pltpu.VMEM)),
        input_output_aliases={k: 2 + k for k in range(n + m)},
        compiler_params=pltpu.CompilerParams(has_side_effects=DATAFLOW),
    )(*[_in_hbm(s) for s in srcs], *[_in_hbm(l) for l in lands], after)
    return out[0], out[1], list(out[2:2 + n]), list(out[2 + n:2 + n + m]), out[-1]


def _split_copy_wait(send_sems, recv_sems, srcs, lands, copies, after, *, name):
    n, m = len(srcs), len(lands)

    def body(*refs):
        src_refs, land_refs = refs[:n], refs[n:n + m]
        send, recv = refs[n + m], refs[n + m + 1]
        x, y, c = _position()
        for k, (src, dst, peer) in enumerate(copies(x, y, c, src_refs, land_refs)):
            cp = pltpu.make_async_remote_copy(src_ref=src, dst_ref=dst, send_sem=send.at[k], recv_sem=recv.at[k],
                                              device_id=peer, device_id_type=MESH)
            cp.wait_send()
            cp.wait_recv()

    afters = list(after) if isinstance(after, (list, tuple)) else [after]
    out = _pcall(
        body, name=name,
        out_shape=tuple(pltpu.HBM(s.shape, s.dtype) for s in list(srcs) + list(lands)),
        in_specs=[HBM_SPEC] * (n + m) + [SEM_SPEC, SEM_SPEC] + [ANY_SPEC] * len(afters),
        out_specs=tuple([HBM_SPEC] * (n + m)),
        input_output_aliases={k: k for k in range(n + m)},
        compiler_params=pltpu.CompilerParams(has_side_effects=DATAFLOW),
    )(*srcs, *lands, send_sems, recv_sems, *afters)
    return list(out[:n]), list(out[n:])


def _gather_copies(x, y, c, shard_refs, land_refs):
    out = []
    for s, l in zip(shard_refs, land_refs):
        half = s.shape[0] // 2
        nq = _row_chunks(half)
        for cx, cy in _other_chips(x, y):
            for q in range(nq):
                rows = pl.ds(c * half + q * (half // nq), half // nq)
                out.append((s.at[rows, :], l.at[2 * x + y, rows, :], (cx, cy, c)))
    return out


def _gather_whole_copies(x, y, c, shard_refs, land_refs):
    out = []
    for s, l in zip(shard_refs, land_refs):
        nq = _row_chunks(s.shape[0])
        for cx, cy in _other_chips(x, y):
            for q in range(nq):
                rows = pl.ds(q * (s.shape[0] // nq), s.shape[0] // nq)
                out.append((s.at[rows, :], l.at[2 * x + y, rows, :], (cx, cy, c)))
    return out


def _scatter_all_copies(x, y, c, block_refs, land_refs):
    out = []
    for g, l in zip(block_refs, land_refs):
        half = g.shape[1] // 2
        nq = _row_chunks(half)
        size = half // nq
        for q in range(nq):
            rows = pl.ds((1 - c) * half + q * size, size)
            out.append((g.at[2 * x + y, rows, :], l.at[0, pl.ds(q * size, size), :], (x, y, 1 - c)))
        for r, (cx, cy) in enumerate(_other_chips(x, y)):
            for j in range(2):
                h = c if j == 0 else 1 - c
                for q in range(nq):
                    rows = pl.ds(h * half + q * size, size)
                    out.append((g.at[2 * cx + cy, rows, :], l.at[1 + 2 * r + j, pl.ds(q * size, size), :], (cx, cy, h)))
    return out


def _forward_halves(lands, *, name):
    n = len(lands)

    def body(*refs):
        ins = refs[:n]
        send_sems, recv_sems = refs[2 * n:]
        x, y, c = _position()
        copies = []
        for w in range(n):
            half = ins[w].shape[1] // 2
            for r, (cx, cy) in enumerate(_other_chips(x, y)):
                blk = ins[w].at[2 * cx + cy, pl.ds(c * half, half), :]
                cp = pltpu.make_async_remote_copy(src_ref=blk, dst_ref=blk, send_sem=send_sems.at[w, r],
                                                  recv_sem=recv_sems.at[w, r], device_id=(x, y, 1 - c),
                                                  device_id_type=MESH)
                cp.start()
                copies.append(cp)
        for w in range(n):
            half = ins[w].shape[1] // 2
            for r, (cx, cy) in enumerate(_other_chips(x, y)):
                blk = ins[w].at[2 * cx + cy, pl.ds((1 - c) * half, half), :]
                pltpu.make_async_remote_copy(src_ref=blk, dst_ref=blk, send_sem=send_sems.at[w, r],
                                             recv_sem=recv_sems.at[w, r], device_id=(x, y, 1 - c),
                                             device_id_type=MESH).wait_recv()
        for cp in copies:
            cp.wait_send()

    return _pcall(
        body, name=name, in_specs=[HBM_SPEC] * n, out_specs=[HBM_SPEC] * n,
        out_shape=[jax.ShapeDtypeStruct(l.shape, l.dtype) for l in lands],
        input_output_aliases={k: k for k in range(n)},
        scratch_shapes=[pltpu.SemaphoreType.DMA((n, 3)), pltpu.SemaphoreType.DMA((n, 3))],
    )(*lands)


def _share_halves(halves, *, name):
    n = len(halves)

    def body(*refs):
        ins, outs = refs[:n], refs[n:2 * n]
        send_sems, recv_sems = refs[2 * n:]
        x, y, c = _position()
        copies = []
        for w in range(n):
            cp = pltpu.make_async_remote_copy(src_ref=ins[w], dst_ref=outs[w], send_sem=send_sems.at[w],
                                              recv_sem=recv_sems.at[w], device_id=(x, y, 1 - c), device_id_type=MESH)
            cp.start()
            copies.append(cp)
        for cp in copies:
            cp.wait()

    return _pcall(
        body, name=name, in_specs=[HBM_SPEC] * n, out_specs=[HBM_SPEC] * n,
        out_shape=[jax.ShapeDtypeStruct(h.shape, h.dtype) for h in halves],
        scratch_shapes=[pltpu.SemaphoreType.DMA((n,)), pltpu.SemaphoreType.DMA((n,))],
    )(*halves)


def _sum_small(part, after=None):
    rows, width = part.shape

    def body(x_ref, out_ref, all_ref, send_sems, recv_sems):
        x, y, c = _position()
        me, sibling = (x, y, c), (x, y, 1 - c)
        chips = _other_chips(x, y)

        def block(px, py, pc):
            return all_ref.at[pl.ds((4 * px + 2 * py + pc) * rows, rows), :]

        def copy(k, blk, to, src=None):
            return pltpu.make_async_remote_copy(
                src_ref=block(*blk) if src is None else src, dst_ref=block(*blk), send_sem=send_sems.at[k],
                recv_sem=recv_sems.at[k], device_id=to, device_id_type=MESH)

        all_ref[pl.ds((4 * x + 2 * y + c) * rows, rows), :] = x_ref[...]
        first = [copy(0, me, sibling, src=x_ref)]
        first += [copy(1 + j, me, (*chip, c), src=x_ref) for j, chip in enumerate(chips)]
        for cp in first:
            cp.start()
        passed = [copy(4 + j, (*chip, c), sibling) for j, chip in enumerate(chips)]
        for j, chip in enumerate(chips):
            copy(1 + j, (*chip, c), me).wait_recv()
            passed[j].start()
        copy(0, sibling, me).wait_recv()
        for j, chip in enumerate(chips):
            copy(4 + j, (*chip, 1 - c), me).wait_recv()
        for cp in first + passed:
            cp.wait_send()
        total = all_ref[0:rows, :]
        for d in range(1, 8):
            total = total + all_ref[d * rows:(d + 1) * rows, :]
        out_ref[...] = total

    vm = pl.BlockSpec(memory_space=pltpu.VMEM)
    return _pcall(
        body, after, name="sum_small", in_specs=[vm], out_specs=vm, out_shape=jax.ShapeDtypeStruct((rows, width), F32),
        scratch_shapes=[pltpu.VMEM((8 * rows, width), F32), pltpu.SemaphoreType.DMA((7,)), pltpu.SemaphoreType.DMA((7,))],
    )(part)


def _row_tile(R, C, itemsize=4, budget=1 << 20):
    fits = [t for t in range(ROW_TILE, R + 1, ROW_TILE) if R % t == 0 and t * C * itemsize <= budget]
    return max(fits) if fits else R


def _add_all(g, recv, where, *, name):
    _, R, C = g.shape
    half = R // 2
    t = _row_tile(half, C)
    nb = half // t

    def body(w_ref, g_ref, r_ref, o_ref):
        total = g_ref[0].astype(F32)
        for k in range(7):
            total = total + r_ref[k].astype(F32)
        o_ref[...] = total

    grid_spec = pltpu.PrefetchScalarGridSpec(
        num_scalar_prefetch=1, grid=(nb,),
        in_specs=[pl.BlockSpec((1, t, C), lambda i, wr: (wr[0], wr[1] * nb + i, 0)),
                  pl.BlockSpec((7, t, C), lambda i, wr: (0, i, 0))],
        out_specs=pl.BlockSpec((t, C), lambda i, wr: (i, 0)))
    return _pcall(body, name=name, grid_spec=grid_spec, out_shape=jax.ShapeDtypeStruct((half, C), F32),
                  compiler_params=_params("parallel"))(where, g, recv)


def _adamw(w, g, m, v, *, name):
    R, C = w.shape
    t = _row_tile(R, C)
    c1 = 1.0 - ADAM_B1 ** ADAM_STEP
    c2 = 1.0 - ADAM_B2 ** ADAM_STEP

    def body(w_ref, g_ref, m_ref, v_ref, d_ref, nm_ref, nv_ref):
        gv = g_ref[...]
        mn = ADAM_B1 * m_ref[...] + (1.0 - ADAM_B1) * gv
        vn = ADAM_B2 * v_ref[...] + (1.0 - ADAM_B2) * (gv * gv)
        d_ref[...] = -ADAM_LR * ((mn / c1) / (jnp.sqrt(vn / c2) + ADAM_EPS) + ADAM_WD * w_ref[...])
        nm_ref[...] = mn
        nv_ref[...] = vn

    blk = pl.BlockSpec((t, C), lambda i: (i, 0))
    shp = jax.ShapeDtypeStruct((R, C), F32)
    return _pcall(body, name=name, grid=(R // t,), in_specs=[blk] * 4, out_specs=[blk] * 3, out_shape=[shp] * 3,
                  compiler_params=_params("parallel"))(w, g, m, v)


def _adamw_halves(w, mine, theirs, m, v, core, *, name, after=None):
    R, C = w.shape
    half = R // 2
    t = _row_tile(half, C)
    nbh = half // t
    c1 = 1.0 - ADAM_B1 ** ADAM_STEP
    c2 = 1.0 - ADAM_B2 ** ADAM_STEP

    def body(core_ref, w_ref, a_ref, b_ref, m_ref, v_ref, *rest):
        g_ref, d_ref, nm_ref, nv_ref = rest[-4:]
        gv = jnp.where(pl.program_id(0) // nbh == core_ref[0], a_ref[...], b_ref[...])
        mn = ADAM_B1 * m_ref[...] + (1.0 - ADAM_B1) * gv
        vn = ADAM_B2 * v_ref[...] + (1.0 - ADAM_B2) * (gv * gv)
        g_ref[...] = gv
        d_ref[...] = -ADAM_LR * ((mn / c1) / (jnp.sqrt(vn / c2) + ADAM_EPS) + ADAM_WD * w_ref[...])
        nm_ref[...] = mn
        nv_ref[...] = vn

    blk = pl.BlockSpec((t, C), lambda i, cr: (i, 0))
    hblk = pl.BlockSpec((t, C), lambda i, cr: (i % nbh, 0))
    shp = jax.ShapeDtypeStruct((R, C), F32)
    tied = [] if after is None else [after]
    grid_spec = pltpu.PrefetchScalarGridSpec(num_scalar_prefetch=1, grid=(2 * nbh,),
                                             in_specs=[blk, hblk, hblk, blk, blk] + [ANY_SPEC] * len(tied),
                                             out_specs=[blk] * 4)
    return _pcall(body, name=name, grid_spec=grid_spec, out_shape=[shp] * 4,
                  compiler_params=_params("parallel"))(core, w, mine, theirs, m, v, *tied)


BIG = ("w_in", "w_dil_out", "w_fox_out", "w_out", "w_ffn_in", "w_ffn_down")
SMALL = ("norm_mix_g", "b_fgt", "b_gate", "norm_ffn_g", "norm_final_g")
ORDER = ("norm_mix_g", "w_in", "b_fgt", "b_gate", "w_dil_out", "w_fox_out", "w_out", "norm_ffn_g", "w_ffn_in",
         "w_ffn_down", "norm_final_g")
SMALL_ROWS = {"norm_mix_g": (0, 1), "b_gate": (1, 3), "norm_ffn_g": (3, 4), "norm_final_g": (4, 5), "b_fgt": (5, 6)}


def _columns_to_blocks(full, ncol):
    K = full.shape[0]
    return full.reshape(K, 4, ncol).transpose(1, 0, 2)


def _pieces_to_blocks(pieces, ncol):
    spans, start = [], 0
    for piece in pieces:
        spans.append((piece, start, start + piece.shape[1]))
        start += piece.shape[1]
    assert start == 4 * ncol
    blocks = []
    for k in range(4):
        lo, hi = k * ncol, (k + 1) * ncol
        parts = [p[:, max(lo, a) - a:min(hi, b) - a] for p, a, b in spans if a < hi and b > lo]
        blocks.append(parts[0] if len(parts) == 1 else jnp.concatenate(parts, axis=1))
    return jnp.stack(blocks)


def _blocks_to_pieces(blocks, widths):
    n, K, ncol = blocks.shape
    assert sum(widths) == n * ncol
    pieces, lo = [], 0
    for width in widths:
        hi = lo + width
        parts = [blocks[k][:, max(lo, k * ncol) - k * ncol:min(hi, (k + 1) * ncol) - k * ncol]
                 for k in range(n) if k * ncol < hi and (k + 1) * ncol > lo]
        pieces.append(parts[0] if len(parts) == 1 else jnp.concatenate(parts, axis=1))
        lo = hi
    return pieces


def _blocks_to_columns(blocks):
    n, K, ncol = blocks.shape
    return blocks.transpose(1, 0, 2).reshape(K, n * ncol)


def kernel(x, norm_mix_g, w_in, b_fgt, b_gate, w_dil_out, w_fox_out, w_out, norm_ffn_g, w_ffn_in, w_ffn_down, norm_final_g, loss_target, m_norm_mix_g, m_w_in, m_b_fgt, m_b_gate, m_w_dil_out, m_w_fox_out, m_w_out, m_norm_ffn_g, m_w_ffn_in, m_w_ffn_down, m_norm_final_g, v_norm_mix_g, v_w_in, v_b_fgt, v_b_gate, v_w_dil_out, v_w_fox_out, v_w_out, v_norm_ffn_g, v_w_ffn_in, v_w_ffn_down, v_norm_final_g):
    weights = dict(norm_mix_g=norm_mix_g, w_in=w_in, b_fgt=b_fgt, b_gate=b_gate, w_dil_out=w_dil_out,
                   w_fox_out=w_fox_out, w_out=w_out, norm_ffn_g=norm_ffn_g, w_ffn_in=w_ffn_in, w_ffn_down=w_ffn_down,
                   norm_final_g=norm_final_g)
    m_in = dict(norm_mix_g=m_norm_mix_g, w_in=m_w_in, b_fgt=m_b_fgt, b_gate=m_b_gate, w_dil_out=m_w_dil_out,
                w_fox_out=m_w_fox_out, w_out=m_w_out, norm_ffn_g=m_norm_ffn_g, w_ffn_in=m_w_ffn_in,
                w_ffn_down=m_w_ffn_down, norm_final_g=m_norm_final_g)
    v_in = dict(norm_mix_g=v_norm_mix_g, w_in=v_w_in, b_fgt=v_b_fgt, b_gate=v_b_gate, w_dil_out=v_w_dil_out,
                w_fox_out=v_w_fox_out, w_out=v_w_out, norm_ffn_g=v_norm_ffn_g, w_ffn_in=v_w_ffn_in,
                w_ffn_down=v_w_ffn_down, norm_final_g=v_norm_final_g)
    c = lax.axis_index("c")
    chip = 2 * lax.axis_index("x") + lax.axis_index("y")

    shards = {n: weights[n][0].astype(_CD) for n in BIG}
    in_shape = jax.ShapeDtypeStruct((4,) + shards["w_in"].shape, _CD)
    send_i, recv_i, in_src, in_land, token_in = _split_copy_start(
        [shards["w_in"]], [in_shape], _gather_copies, norm_mix_g, name="gather_in_start")
    late = BIG[1:]
    send_g, recv_g, late_src, late_land, token = _split_copy_start(
        [shards[n] for n in late], [jax.ShapeDtypeStruct((4,) + shards[n].shape, _CD) for n in late],
        _gather_whole_copies, token_in, name="gather_late_start")
    adam_in = [t[0] + token_in[0, 0] for t in (w_in, m_w_in, v_w_in)]
    p = dict(norm_mix_g=norm_mix_g, b_fgt=jnp.pad(b_fgt, ((0, 0), (0, F_PAD - N_FOX_HEADS))), b_gate=b_gate,
             norm_ffn_g=norm_ffn_g, norm_final_g=norm_final_g.reshape(1, D_MODEL))

    def first_weights(after):
        own, lands = _split_copy_wait(send_i, recv_i, in_src, in_land, _gather_copies, [after] + adam_in,
                                      name="gather_in_wait")
        (g_in,) = _forward_halves(lands, name="gather_in_forward")
        blocks = lax.dynamic_update_index_in_dim(g_in, own[0], chip, 0)
        qkv, f, g = _blocks_to_pieces(blocks, (QKV_COLS, N_FOX_HEADS, 2 * D_MODEL))
        return dict(qkv=qkv, f=jnp.pad(f, ((0, 0), (0, F_PAD - N_FOX_HEADS))), g=g)

    def late_weights(after):
        own, lands = _split_copy_wait(send_g, recv_g, late_src, late_land, _gather_whole_copies, after,
                                      name="gather_late_wait")
        g_dil, g_fox, g_out, g_ffn_in, g_ffn_down = [
            lax.dynamic_update_index_in_dim(l, s, chip, 0) for l, s in zip(lands, own)]
        return dict(dil_out=_blocks_to_columns(g_dil), fox_out=_blocks_to_columns(g_fox),
                    out=g_out.reshape(D_MODEL, D_MODEL), ffn_in=g_ffn_in,
                    ffn_down=g_ffn_down.reshape(D_FF, D_MODEL))

    def to_blocks(n, full):
        shape = weights[n].shape
        if full.ndim == 3:
            return full
        if n in ("w_out", "w_ffn_down"):
            return full.reshape(4, shape[1], shape[2])
        return _columns_to_blocks(full, shape[2])

    in_flight = {}

    def grad_sink(group, gw):
        if group == "in":
            named = {"w_in": _pieces_to_blocks([gw["qkv"], gw["f"][:, :N_FOX_HEADS], gw["g"]], weights["w_in"].shape[2])}
        else:
            named = {"w_" + k: v for k, v in gw.items()}
        srcs = [to_blocks(n, named[n]) for n in named]
        lands = [jax.ShapeDtypeStruct((7, s.shape[1] // 2, s.shape[2]), s.dtype) for s in srcs]
        started = _split_copy_start(srcs, lands, _scatter_all_copies, next(iter(gw.values())),
                                    name=f"scatter_{group}_start")
        in_flight[group] = (list(named), started)
        return started[-1]

    loss_part, grad_x, gw, small = _layer_step(x[0], loss_target[0], {}, p, late_weights, grad_sink,
                                               (token_in, token), first_weights)

    where = jnp.stack([chip, c]).astype(jnp.int32)

    def summed_halves(groups, after, name):
        halves = {}
        for group in groups:
            names, (send_s, recv_s, srcs, lands, _) = in_flight[group]
            srcs, recv = _split_copy_wait(send_s, recv_s, srcs, lands, _scatter_all_copies, after,
                                          name=f"scatter_{group}_wait")
            halves.update({n: _add_all(s, r, where, name=f"add_all_{n}") for n, s, r in zip(names, srcs, recv)})
        return {n: (h, o) for (n, h), o in zip(halves.items(), _share_halves(list(halves.values()), name=name))}

    grad_halves = summed_halves(("ffn", "mix"), grad_x, "share_halves")

    out_g, out_d, out_m, out_v = {}, {}, {}, {}
    core = jnp.reshape(c, (1,)).astype(jnp.int32)

    def adamw_big(n, after=None):
        shape = weights[n].shape
        wmv = adam_in if n == "w_in" else [t[0] for t in (weights[n], m_in[n], v_in[n])]
        mine, theirs = grad_halves[n]
        outs = _adamw_halves(wmv[0], mine, theirs, wmv[1], wmv[2], core, name=f"adamw_{n}", after=after)
        out_g[n], out_d[n], out_m[n], out_v[n] = [t.reshape(shape) for t in outs]

    early = ("w_dil_out", "w_fox_out", "w_out", "w_ffn_in")
    for n in early:
        adamw_big(n)
    grad_halves.update(summed_halves(("in",), [grad_x] + [out_d[n] for n in early], "share_halves_in"))
    adamw_big("w_in")

    packed = jnp.concatenate([
        small["norm_mix_g"], small["b_gate"].reshape(2, D_MODEL), small["norm_ffn_g"], small["norm_final_g"],
        jnp.pad(small["b_fgt"], ((0, 0), (0, D_MODEL - F_PAD))), jnp.pad(loss_part, ((0, 0), (0, D_MODEL - 1))),
        jnp.zeros((1, D_MODEL), F32)], axis=0)
    summed = _sum_small(packed, after=out_d["w_in"])
    loss = summed[6, 0]
    adamw_big("w_ffn_down", after=summed)

    for n in SMALL:
        lo, hi = SMALL_ROWS[n]
        shape = weights[n].shape
        g2 = summed[lo:hi].reshape(1, -1)[:, :weights[n].size]
        d2, m2, v2 = _adamw(weights[n].reshape(g2.shape), g2, m_in[n].reshape(g2.shape), v_in[n].reshape(g2.shape),
                            name=f"adamw_{n}")
        out_g[n], out_d[n], out_m[n], out_v[n] = [t.reshape(shape) for t in (g2, d2, m2, v2)]
    return (loss, grad_x[None], *[out_g[n] for n in ORDER], *[out_d[n] for n in ORDER],
            *[out_m[n] for n in ORDER], *[out_v[n] for n in ORDER])
```
